```python
import jax, jax.numpy as jnp
from jax import lax
import numpy as np

D_MODEL = 1024
BATCH = 8
SEQ = 2048
DEPTH = 1

GRID_W = 64
N_ATT_HEADS = 8
ATT_HEAD_DIM = 64
D_ATT = N_ATT_HEADS * ATT_HEAD_DIM
WIN_H_MAX = 8
WIN_W = 16
Q_BLOCK_W = 16
KV_BLOCK_W = Q_BLOCK_W + WIN_W
N_COL_BLOCKS = GRID_W // Q_BLOCK_W
D_REC = D_MODEL
N_REC_BLOCKS = 16
REC_BLOCK = D_REC // N_REC_BLOCKS
CONV_W = 4
LRU_C = 8.0
N_DIR = 2
D_FF = 4 * D_MODEL
EPS = 1e-6
D_IN = 3 * D_ATT + 2 * D_REC + 2 * D_MODEL
SPLITS = [int(v) for v in np.cumsum([D_ATT, D_ATT, D_ATT, D_REC, D_REC, D_MODEL])]

kernel_name = "hybrid_natten_rglru_gated_encoder"


def rms_norm(x, g):
    x32 = x.astype(jnp.float32)
    y = x32 * lax.rsqrt(jnp.mean(x32 * x32, axis=-1, keepdims=True) + EPS)
    return (y * g.astype(jnp.float32)).astype(x.dtype)


def neighbourhood_attention(q, k, v, rpb):
    b, s, h, dh = q.shape
    rows = s // GRID_W
    kh = min(WIN_H_MAX, rows)
    r = np.arange(rows)
    row_start = np.clip(r - kh // 2, 0, rows - kh)
    row_idx = row_start[:, None] + np.arange(kh)[None, :]
    n = np.arange(N_COL_BLOCKS)
    col_start = np.clip(n * Q_BLOCK_W - WIN_W // 2, 0, GRID_W - KV_BLOCK_W)
    col_idx = col_start[:, None] + np.arange(KV_BLOCK_W)[None, :]
    qc = n[:, None] * Q_BLOCK_W + np.arange(Q_BLOCK_W)[None, :]
    win_start = np.clip(qc - WIN_W // 2, 0, GRID_W - WIN_W)
    kc = col_idx[:, None, :]
    valid = (kc >= win_start[..., None]) & (kc < win_start[..., None] + WIN_W)
    d_row = row_idx - r[:, None] + (WIN_H_MAX - 1)
    d_col = np.clip(kc - qc[..., None], -(WIN_W - 1), WIN_W - 1) + (WIN_W - 1)

    scale = ATT_HEAD_DIM ** -0.5
    q_blk = (q * scale).reshape(b, rows, N_COL_BLOCKS, Q_BLOCK_W, h, dh).transpose(0, 4, 1, 2, 3, 5)
    k_grid = k.reshape(b, rows, GRID_W, h, dh).transpose(0, 3, 1, 2, 4)
    v_grid = v.reshape(b, rows, GRID_W, h, dh).transpose(0, 3, 1, 2, 4)
    ri = row_idx[:, None, :, None]
    ci = col_idx[None, :, None, :]
    kg = k_grid[:, :, ri, ci]
    vg = v_grid[:, :, ri, ci]

    scores = jnp.einsum('bhrnqd,bhrnikd->bhrnqik', q_blk, kg).astype(jnp.float32)
    bias = rpb.astype(jnp.float32)[:, d_row[:, None, None, :, None], d_col[None, :, :, None, :]]
    scores = scores + bias[None]
    scores = jnp.where(valid[:, :, None, :], scores, -1e30)
    probs = jax.nn.softmax(scores, axis=(-2, -1)).astype(v.dtype)
    out = jnp.einsum('bhrnqik,bhrnikd->bhrnqd', probs, vg)
    return out.transpose(0, 2, 3, 4, 1, 5).reshape(b, s, h * dh)


def centred_depthwise_conv(u, w, bias):
    s = u.shape[1]
    left = CONV_W // 2
    right = CONV_W - 1 - left
    up = jnp.pad(u, ((0, 0), (left, right), (0, 0)))
    out = bias
    for j in range(CONV_W):
        out = out + up[:, j:j + s] * w[j]
    return out


def block_diag_linear(u, w, b):
    bsz, s, c = u.shape
    ub = u.reshape(bsz, s, N_REC_BLOCKS, REC_BLOCK)
    return jnp.einsum('bsnc,ncd->bsnd', ub, w).reshape(bsz, s, c) + b


def rg_lru(u, w_a, b_a, w_i, b_i, lam, reverse):
    r_gate = jax.nn.sigmoid(block_diag_linear(u, w_a, b_a)).astype(jnp.float32)
    i_gate = jax.nn.sigmoid(block_diag_linear(u, w_i, b_i))
    log_a = -LRU_C * r_gate * jax.nn.softplus(-lam.astype(jnp.float32))
    a = jnp.exp(log_a)
    mult = jnp.sqrt(jnp.maximum(-jnp.expm1(2.0 * log_a), 0.0))
    bx = mult * (i_gate * u).astype(jnp.float32)

    def combine(c1, c2):
        a1, b1 = c1
        a2, b2 = c2
        return a1 * a2, a2 * b1 + b2

    _, h = lax.associative_scan(combine, (a, bx), axis=1, reverse=reverse)
    return h.astype(u.dtype)


def _fwd_setup_inputs(seed: int = 0) -> dict:
    key = jax.random.key(seed)
    ks = jax.random.split(key, 20)
    f32 = jnp.float32
    nrm = lambda k, shape, fan_in: jax.random.normal(k, shape, f32) * (fan_in ** -0.5)
    x = jax.random.normal(ks[0], (BATCH, SEQ, D_MODEL), f32)
    ln1_g = 1.0 + 0.05 * jax.random.normal(ks[1], (DEPTH, D_MODEL), f32)
    w_in = nrm(ks[2], (DEPTH, D_MODEL, D_IN), D_MODEL)
    b_in = 0.02 * jax.random.normal(ks[3], (DEPTH, D_IN), f32)
    rpb = 0.02 * jax.random.normal(ks[4], (DEPTH, N_ATT_HEADS, 2 * WIN_H_MAX - 1, 2 * WIN_W - 1), f32)
    w_att_o = nrm(ks[5], (DEPTH, D_ATT, D_MODEL), D_ATT)
    conv_w = nrm(ks[6], (DEPTH, CONV_W, D_REC), CONV_W)
    conv_b = 0.02 * jax.random.normal(ks[7], (DEPTH, D_REC), f32)
    w_rg_a = nrm(ks[8], (DEPTH, N_DIR, N_REC_BLOCKS, REC_BLOCK, REC_BLOCK), REC_BLOCK)
    b_rg_a = 0.02 * jax.random.normal(ks[9], (DEPTH, N_DIR, D_REC), f32)
    w_rg_i = nrm(ks[10], (DEPTH, N_DIR, N_REC_BLOCKS, REC_BLOCK, REC_BLOCK), REC_BLOCK)
    b_rg_i = 0.02 * jax.random.normal(ks[11], (DEPTH, N_DIR, D_REC), f32)
    a_c = jax.random.uniform(ks[12], (DEPTH, N_DIR, D_REC), f32, 0.9, 0.999)
    a0 = a_c ** (1.0 / LRU_C)
    lru_lambda = jnp.log(a0) - jnp.log1p(-a0)
    w_rec_o = nrm(ks[13], (DEPTH, D_REC, D_MODEL), D_REC)
    w_out = nrm(ks[14], (DEPTH, D_MODEL, D_MODEL), D_MODEL)
    ln2_g = 1.0 + 0.05 * jax.random.normal(ks[15], (DEPTH, D_MODEL), f32)
    w_ff1 = nrm(ks[16], (DEPTH, D_MODEL, D_FF), D_MODEL)
    w_ff2 = nrm(ks[17], (DEPTH, D_FF, D_MODEL), D_FF)
    lnf_g = 1.0 + 0.05 * jax.random.normal(ks[18], (D_MODEL,), f32)
    return {"x": x, "ln1_g": ln1_g, "w_in": w_in, "b_in": b_in, "rpb": rpb,
            "w_att_o": w_att_o, "conv_w": conv_w, "conv_b": conv_b,
            "w_rg_a": w_rg_a, "b_rg_a": b_rg_a, "w_rg_i": w_rg_i, "b_rg_i": b_rg_i,
            "lru_lambda": lru_lambda, "w_rec_o": w_rec_o, "w_out": w_out,
            "ln2_g": ln2_g, "w_ff1": w_ff1, "w_ff2": w_ff2, "lnf_g": lnf_g}


def _fwd_reference(x, ln1_g, w_in, b_in, rpb, w_att_o, conv_w, conv_b, w_rg_a, b_rg_a,
              w_rg_i, b_rg_i, lru_lambda, w_rec_o, w_out, ln2_g, w_ff1, w_ff2, lnf_g):
    b, s, _ = x.shape
    for l in range(DEPTH):
        h = rms_norm(x, ln1_g[l])
        z = h @ w_in[l] + b_in[l]
        q, k, v, u, y_branch, g_att, g_rec = jnp.split(z, SPLITS, axis=-1)

        q = q.reshape(b, s, N_ATT_HEADS, ATT_HEAD_DIM)
        k = k.reshape(b, s, N_ATT_HEADS, ATT_HEAD_DIM)
        v = v.reshape(b, s, N_ATT_HEADS, ATT_HEAD_DIM)
        y_att = neighbourhood_attention(q, k, v, rpb[l]) @ w_att_o[l]

        u = centred_depthwise_conv(u, conv_w[l], conv_b[l])
        h_fwd = rg_lru(u, w_rg_a[l, 0], b_rg_a[l, 0], w_rg_i[l, 0], b_rg_i[l, 0], lru_lambda[l, 0], False)
        h_bwd = rg_lru(u, w_rg_a[l, 1], b_rg_a[l, 1], w_rg_i[l, 1], b_rg_i[l, 1], lru_lambda[l, 1], True)
        y_rec = ((h_fwd + h_bwd) * jax.nn.gelu(y_branch)) @ w_rec_o[l]

        mixed = jax.nn.sigmoid(g_att) * y_att + jax.nn.sigmoid(g_rec) * y_rec
        x = x + mixed @ w_out[l]

        h2 = rms_norm(x, ln2_g[l])
        x = x + jnp.square(jax.nn.relu(h2 @ w_ff1[l])) @ w_ff2[l]
    return rms_norm(x, lnf_g)


import jax as _jax
import jax.numpy as _jnp

TWIN_FORMAT = 'train_step'
FWD_PARAMS = ['x', 'ln1_g', 'w_in', 'b_in', 'rpb', 'w_att_o', 'conv_w', 'conv_b', 'w_rg_a', 'b_rg_a', 'w_rg_i', 'b_rg_i', 'lru_lambda', 'w_rec_o', 'w_out', 'ln2_g', 'w_ff1', 'w_ff2', 'lnf_g']
TWIN_WEIGHTS = ['ln1_g', 'w_in', 'b_in', 'rpb', 'w_att_o', 'conv_w', 'conv_b', 'w_rg_a', 'b_rg_a', 'w_rg_i', 'b_rg_i', 'lru_lambda', 'w_rec_o', 'w_out', 'ln2_g', 'w_ff1', 'w_ff2', 'lnf_g']
TWIN_DIFF_INPUT = 'x'
TWIN_INPUTS = ['x', 'ln1_g', 'w_in', 'b_in', 'rpb', 'w_att_o', 'conv_w', 'conv_b', 'w_rg_a', 'b_rg_a', 'w_rg_i', 'b_rg_i', 'lru_lambda', 'w_rec_o', 'w_out', 'ln2_g', 'w_ff1', 'w_ff2', 'lnf_g', 'loss_target', 'm_ln1_g', 'm_w_in', 'm_b_in', 'm_rpb', 'm_w_att_o', 'm_conv_w', 'm_conv_b', 'm_w_rg_a', 'm_b_rg_a', 'm_w_rg_i', 'm_b_rg_i', 'm_lru_lambda', 'm_w_rec_o', 'm_w_out', 'm_ln2_g', 'm_w_ff1', 'm_w_ff2', 'm_lnf_g', 'v_ln1_g', 'v_w_in', 'v_b_in', 'v_rpb', 'v_w_att_o', 'v_conv_w', 'v_conv_b', 'v_w_rg_a', 'v_b_rg_a', 'v_w_rg_i', 'v_b_rg_i', 'v_lru_lambda', 'v_w_rec_o', 'v_w_out', 'v_ln2_g', 'v_w_ff1', 'v_w_ff2', 'v_lnf_g']
TWIN_OUTPUTS = ['loss', 'grad_x', 'grad_ln1_g', 'grad_w_in', 'grad_b_in', 'grad_rpb', 'grad_w_att_o', 'grad_conv_w', 'grad_conv_b', 'grad_w_rg_a', 'grad_b_rg_a', 'grad_w_rg_i', 'grad_b_rg_i', 'grad_lru_lambda', 'grad_w_rec_o', 'grad_w_out', 'grad_ln2_g', 'grad_w_ff1', 'grad_w_ff2', 'grad_lnf_g', 'delta_ln1_g', 'delta_w_in', 'delta_b_in', 'delta_rpb', 'delta_w_att_o', 'delta_conv_w', 'delta_conv_b', 'delta_w_rg_a', 'delta_b_rg_a', 'delta_w_rg_i', 'delta_b_rg_i', 'delta_lru_lambda', 'delta_w_rec_o', 'delta_w_out', 'delta_ln2_g', 'delta_w_ff1', 'delta_w_ff2', 'delta_lnf_g', 'new_m_ln1_g', 'new_m_w_in', 'new_m_b_in', 'new_m_rpb', 'new_m_w_att_o', 'new_m_conv_w', 'new_m_conv_b', 'new_m_w_rg_a', 'new_m_b_rg_a', 'new_m_w_rg_i', 'new_m_b_rg_i', 'new_m_lru_lambda', 'new_m_w_rec_o', 'new_m_w_out', 'new_m_ln2_g', 'new_m_w_ff1', 'new_m_w_ff2', 'new_m_lnf_g', 'new_v_ln1_g', 'new_v_w_in', 'new_v_b_in', 'new_v_rpb', 'new_v_w_att_o', 'new_v_conv_w', 'new_v_conv_b', 'new_v_w_rg_a', 'new_v_b_rg_a', 'new_v_w_rg_i', 'new_v_b_rg_i', 'new_v_lru_lambda', 'new_v_w_rec_o', 'new_v_w_out', 'new_v_ln2_g', 'new_v_w_ff1', 'new_v_w_ff2', 'new_v_lnf_g']
TWIN_LEAF_KINDS = {'loss': 'loss', 'grad_x': 'grad_x', 'grad_ln1_g': 'grad_w', 'grad_w_in': 'grad_w', 'grad_b_in': 'grad_w', 'grad_rpb': 'grad_w', 'grad_w_att_o': 'grad_w', 'grad_conv_w': 'grad_w', 'grad_conv_b': 'grad_w', 'grad_w_rg_a': 'grad_w', 'grad_b_rg_a': 'grad_w', 'grad_w_rg_i': 'grad_w', 'grad_b_rg_i': 'grad_w', 'grad_lru_lambda': 'grad_w', 'grad_w_rec_o': 'grad_w', 'grad_w_out': 'grad_w', 'grad_ln2_g': 'grad_w', 'grad_w_ff1': 'grad_w', 'grad_w_ff2': 'grad_w', 'grad_lnf_g': 'grad_w', 'delta_ln1_g': 'delta_w', 'delta_w_in': 'delta_w', 'delta_b_in': 'delta_w', 'delta_rpb': 'delta_w', 'delta_w_att_o': 'delta_w', 'delta_conv_w': 'delta_w', 'delta_conv_b': 'delta_w', 'delta_w_rg_a': 'delta_w', 'delta_b_rg_a': 'delta_w', 'delta_w_rg_i': 'delta_w', 'delta_b_rg_i': 'delta_w', 'delta_lru_lambda': 'delta_w', 'delta_w_rec_o': 'delta_w', 'delta_w_out': 'delta_w', 'delta_ln2_g': 'delta_w', 'delta_w_ff1': 'delta_w', 'delta_w_ff2': 'delta_w', 'delta_lnf_g': 'delta_w', 'new_m_ln1_g': 'new_m', 'new_m_w_in': 'new_m', 'new_m_b_in': 'new_m', 'new_m_rpb': 'new_m', 'new_m_w_att_o': 'new_m', 'new_m_conv_w': 'new_m', 'new_m_conv_b': 'new_m', 'new_m_w_rg_a': 'new_m', 'new_m_b_rg_a': 'new_m', 'new_m_w_rg_i': 'new_m', 'new_m_b_rg_i': 'new_m', 'new_m_lru_lambda': 'new_m', 'new_m_w_rec_o': 'new_m', 'new_m_w_out': 'new_m', 'new_m_ln2_g': 'new_m', 'new_m_w_ff1': 'new_m', 'new_m_w_ff2': 'new_m', 'new_m_lnf_g': 'new_m', 'new_v_ln1_g': 'new_v', 'new_v_w_in': 'new_v', 'new_v_b_in': 'new_v', 'new_v_rpb': 'new_v', 'new_v_w_att_o': 'new_v', 'new_v_conv_w': 'new_v', 'new_v_conv_b': 'new_v', 'new_v_w_rg_a': 'new_v', 'new_v_b_rg_a': 'new_v', 'new_v_w_rg_i': 'new_v', 'new_v_b_rg_i': 'new_v', 'new_v_lru_lambda': 'new_v', 'new_v_w_rec_o': 'new_v', 'new_v_w_out': 'new_v', 'new_v_ln2_g': 'new_v', 'new_v_w_ff1': 'new_v', 'new_v_w_ff2': 'new_v', 'new_v_lnf_g': 'new_v'}


def _forward(args):
    return _fwd_reference(*[args[k] for k in FWD_PARAMS])


def _output_shape():
    out = _jax.eval_shape(lambda: _forward(_fwd_setup_inputs(0)))
    return out.shape, out.dtype

N_MICROBATCH = 1
ADAM_LR = 0.001
ADAM_B1 = 0.9
ADAM_B2 = 0.999
ADAM_EPS = 1e-08
ADAM_WD = 0.01
ADAM_STEP = 10
PER_EXAMPLE_BATCH_AXIS = {'x': 0, 'loss_target': 0}
SHARED_INPUTS = []
_WEIGHT_DTYPES = {'ln1_g': _jnp.float32, 'w_in': _jnp.float32, 'b_in': _jnp.float32, 'rpb': _jnp.float32, 'w_att_o': _jnp.float32, 'conv_w': _jnp.float32, 'conv_b': _jnp.float32, 'w_rg_a': _jnp.float32, 'b_rg_a': _jnp.float32, 'w_rg_i': _jnp.float32, 'b_rg_i': _jnp.float32, 'lru_lambda': _jnp.float32, 'w_rec_o': _jnp.float32, 'w_out': _jnp.float32, 'ln2_g': _jnp.float32, 'w_ff1': _jnp.float32, 'w_ff2': _jnp.float32, 'lnf_g': _jnp.float32}
MOMENT_SCALE = {'ln1_g': 6.984153e-02, 'w_in': 2.921700e-02, 'b_in': 3.822597e-01, 'rpb': 6.870551e-03, 'w_att_o': 1.443887e-02, 'conv_w': 5.426976e-02, 'conv_b': 8.207950e-01, 'w_rg_a': 1.623999e-02, 'b_rg_a': 1.180455e-02, 'w_rg_i': 2.982966e-02, 'b_rg_i': 1.184269e-02, 'lru_lambda': 1.910629e-02, 'w_rec_o': 5.506531e-02, 'w_out': 5.527076e-02, 'ln2_g': 1.107899e-01, 'w_ff1': 5.568195e-02, 'w_ff2': 1.408999e-01, 'lnf_g': 1.619869e+01}


def _to_microbatches(a, axis):
    t = _jnp.moveaxis(a, axis, 0)
    t = t.reshape((N_MICROBATCH, t.shape[0] // N_MICROBATCH) + t.shape[1:])
    return _jnp.moveaxis(t, 1, axis + 1)


def setup_inputs(seed: int = 0) -> dict:
    inp = _fwd_setup_inputs(seed)
    key = _jax.random.fold_in(_jax.random.key(seed), 7919)
    shape, _ = _output_shape()
    out = dict(inp)
    out["loss_target"] = _jax.random.normal(_jax.random.fold_in(key, 0), shape, _jnp.float32)
    for i, name in enumerate(TWIN_WEIGHTS):
        w = inp[name].astype(_jnp.float32)
        if MOMENT_SCALE is None:
            s = _jnp.sqrt(_jnp.mean(_jnp.square(w)) + 1e-30)
        else:
            s = MOMENT_SCALE[name]
        km, kv = _jax.random.split(_jax.random.fold_in(key, i + 1))
        out[name] = w
        out["m_" + name] = s * _jax.random.normal(km, w.shape, _jnp.float32)
        out["v_" + name] = (s * s) * _jax.random.uniform(kv, w.shape, _jnp.float32, 0.5, 1.5)
    if N_MICROBATCH > 1:
        for name, axis in PER_EXAMPLE_BATCH_AXIS.items():
            out[name] = _to_microbatches(out[name], axis)
    return {'x': out['x'], 'ln1_g': out['ln1_g'], 'w_in': out['w_in'], 'b_in': out['b_in'], 'rpb': out['rpb'], 'w_att_o': out['w_att_o'], 'conv_w': out['conv_w'], 'conv_b': out['conv_b'], 'w_rg_a': out['w_rg_a'], 'b_rg_a': out['b_rg_a'], 'w_rg_i': out['w_rg_i'], 'b_rg_i': out['b_rg_i'], 'lru_lambda': out['lru_lambda'], 'w_rec_o': out['w_rec_o'], 'w_out': out['w_out'], 'ln2_g': out['ln2_g'], 'w_ff1': out['w_ff1'], 'w_ff2': out['w_ff2'], 'lnf_g': out['lnf_g'], 'loss_target': out['loss_target'], 'm_ln1_g': out['m_ln1_g'], 'm_w_in': out['m_w_in'], 'm_b_in': out['m_b_in'], 'm_rpb': out['m_rpb'], 'm_w_att_o': out['m_w_att_o'], 'm_conv_w': out['m_conv_w'], 'm_conv_b': out['m_conv_b'], 'm_w_rg_a': out['m_w_rg_a'], 'm_b_rg_a': out['m_b_rg_a'], 'm_w_rg_i': out['m_w_rg_i'], 'm_b_rg_i': out['m_b_rg_i'], 'm_lru_lambda': out['m_lru_lambda'], 'm_w_rec_o': out['m_w_rec_o'], 'm_w_out': out['m_w_out'], 'm_ln2_g': out['m_ln2_g'], 'm_w_ff1': out['m_w_ff1'], 'm_w_ff2': out['m_w_ff2'], 'm_lnf_g': out['m_lnf_g'], 'v_ln1_g': out['v_ln1_g'], 'v_w_in': out['v_w_in'], 'v_b_in': out['v_b_in'], 'v_rpb': out['v_rpb'], 'v_w_att_o': out['v_w_att_o'], 'v_conv_w': out['v_conv_w'], 'v_conv_b': out['v_conv_b'], 'v_w_rg_a': out['v_w_rg_a'], 'v_b_rg_a': out['v_b_rg_a'], 'v_w_rg_i': out['v_w_rg_i'], 'v_b_rg_i': out['v_b_rg_i'], 'v_lru_lambda': out['v_lru_lambda'], 'v_w_rec_o': out['v_w_rec_o'], 'v_w_out': out['v_w_out'], 'v_ln2_g': out['v_ln2_g'], 'v_w_ff1': out['v_w_ff1'], 'v_w_ff2': out['v_w_ff2'], 'v_lnf_g': out['v_lnf_g']}


def _loss(weights, diff, rest, loss_target):
    with _jax.named_scope("forward"):
        args = {**rest, TWIN_DIFF_INPUT: diff, **{k: w.astype(_WEIGHT_DTYPES[k]) for k, w in weights.items()}}
        y = _forward(args)
    with _jax.named_scope("loss_head"):
        err = _jnp.square(y.astype(_jnp.float32) - loss_target)
        return 0.5 * _jnp.sum(_jnp.mean(err, axis=-1)) if err.ndim else 0.5 * err


def _adamw(w, g, m, v):
    m = ADAM_B1 * m + (1.0 - ADAM_B1) * g
    v = ADAM_B2 * v + (1.0 - ADAM_B2) * _jnp.square(g)
    m_hat = m / (1.0 - ADAM_B1 ** ADAM_STEP)
    v_hat = v / (1.0 - ADAM_B2 ** ADAM_STEP)
    delta = -ADAM_LR * (m_hat / (_jnp.sqrt(v_hat) + ADAM_EPS) + ADAM_WD * w)
    return delta, m, v


def reference(x, ln1_g, w_in, b_in, rpb, w_att_o, conv_w, conv_b, w_rg_a, b_rg_a, w_rg_i, b_rg_i, lru_lambda, w_rec_o, w_out, ln2_g, w_ff1, w_ff2, lnf_g, loss_target, m_ln1_g, m_w_in, m_b_in, m_rpb, m_w_att_o, m_conv_w, m_conv_b, m_w_rg_a, m_b_rg_a, m_w_rg_i, m_b_rg_i, m_lru_lambda, m_w_rec_o, m_w_out, m_ln2_g, m_w_ff1, m_w_ff2, m_lnf_g, v_ln1_g, v_w_in, v_b_in, v_rpb, v_w_att_o, v_conv_w, v_conv_b, v_w_rg_a, v_b_rg_a, v_w_rg_i, v_b_rg_i, v_lru_lambda, v_w_rec_o, v_w_out, v_ln2_g, v_w_ff1, v_w_ff2, v_lnf_g):
    given = dict(x=x, ln1_g=ln1_g, w_in=w_in, b_in=b_in, rpb=rpb, w_att_o=w_att_o, conv_w=conv_w, conv_b=conv_b, w_rg_a=w_rg_a, b_rg_a=b_rg_a, w_rg_i=w_rg_i, b_rg_i=b_rg_i, lru_lambda=lru_lambda, w_rec_o=w_rec_o, w_out=w_out, ln2_g=ln2_g, w_ff1=w_ff1, w_ff2=w_ff2, lnf_g=lnf_g, loss_target=loss_target, m_ln1_g=m_ln1_g, m_w_in=m_w_in, m_b_in=m_b_in, m_rpb=m_rpb, m_w_att_o=m_w_att_o, m_conv_w=m_conv_w, m_conv_b=m_conv_b, m_w_rg_a=m_w_rg_a, m_b_rg_a=m_b_rg_a, m_w_rg_i=m_w_rg_i, m_b_rg_i=m_b_rg_i, m_lru_lambda=m_lru_lambda, m_w_rec_o=m_w_rec_o, m_w_out=m_w_out, m_ln2_g=m_ln2_g, m_w_ff1=m_w_ff1, m_w_ff2=m_w_ff2, m_lnf_g=m_lnf_g, v_ln1_g=v_ln1_g, v_w_in=v_w_in, v_b_in=v_b_in, v_rpb=v_rpb, v_w_att_o=v_w_att_o, v_conv_w=v_conv_w, v_conv_b=v_conv_b, v_w_rg_a=v_w_rg_a, v_b_rg_a=v_b_rg_a, v_w_rg_i=v_w_rg_i, v_b_rg_i=v_b_rg_i, v_lru_lambda=v_lru_lambda, v_w_rec_o=v_w_rec_o, v_w_out=v_w_out, v_ln2_g=v_ln2_g, v_w_ff1=v_w_ff1, v_w_ff2=v_w_ff2, v_lnf_g=v_lnf_g)
    weights = {n: given[n] for n in TWIN_WEIGHTS}
    shared = {n: given[n] for n in SHARED_INPUTS}
    per_example = {n: given[n] for n in ['x']}
    grad_fn = _jax.value_and_grad(_loss, argnums=(0, 1))

    def one_microbatch(ex, loss_target):
        ex = dict(ex)
        diff = ex.pop(TWIN_DIFF_INPUT)
        return grad_fn(weights, diff, {**shared, **ex}, loss_target)

    if N_MICROBATCH == 1:
        loss, (grad_w, grad_x) = one_microbatch(per_example, given["loss_target"])
    else:
        def body(carry, xs):
            loss_sum, grad_sum = carry
            l_k, (gw_k, gx_k) = one_microbatch(xs[0], xs[1])
            with _jax.named_scope("update"):
                return (loss_sum + l_k, _jax.tree.map(_jnp.add, grad_sum, gw_k)), gx_k

        init = (_jnp.zeros((), _jnp.float32), _jax.tree.map(_jnp.zeros_like, weights))
        (loss, grad_w), grad_x = _jax.lax.scan(body, init, (per_example, given["loss_target"]))
    with _jax.named_scope("update"):
        delta_w, new_m, new_v = {}, {}, {}
        for n in TWIN_WEIGHTS:
            delta_w[n], new_m[n], new_v[n] = _adamw(weights[n], grad_w[n], given["m_" + n], given["v_" + n])
    return (loss, grad_x, *[grad_w[n] for n in TWIN_WEIGHTS], *[delta_w[n] for n in TWIN_WEIGHTS],
            *[new_m[n] for n in TWIN_WEIGHTS], *[new_v[n] for n in TWIN_WEIGHTS])
```

```python
import functools
import math

import numpy as np
import jax
import jax.numpy as jnp
from jax import lax
from jax.experimental import pallas as pl
from jax.experimental.pallas import tpu as pltpu

F32 = jnp.float32
BF16 = jnp.bfloat16

T = 2048
D = 1024
D_ATT = 512
D_REC = 1024
D_FF = 4096
D_IN = 5632
N_HEADS = 8
DH = 64
GRID_W = 64
ROWS = T // GRID_W
WIN_H = 8
WIN_W = 16
KWIN = WIN_H * GRID_W
N_RPB_R = 2 * WIN_H - 1
N_RPB_C = 2 * WIN_W - 1
N_REC_BLOCKS = 16
REC_BLOCK = 64
CG = 128
N_CG = D_REC // CG
LRU_C = 8.0
EPS = 1e-6
N_DEV = 8
N_CHIPS = 4

ADAM_LR = 0.001
ADAM_B1 = 0.9
ADAM_B2 = 0.999
ADAM_EPS = 1e-08
ADAM_WD = 0.01
ADAM_STEP = 10

MESH_AXES = ("x", "y", "c")
VMEM_LIMIT = 56 * 1024 * 1024

PACK_ROWS = (("w_in", 704), ("w_att_o", 64), ("w_rec_o", 128), ("w_out", 128),
             ("w_ff1", 512), ("w_ff2", 512), ("chan", 16))
PACK_R = sum(r for _, r in PACK_ROWS)
N_CHAN_ROWS = 10


def _params(**kw):
    return pltpu.CompilerParams(vmem_limit_bytes=VMEM_LIMIT, **kw)


def _att_tables():
    rq = np.arange(2 * GRID_W) % GRID_W
    kc = np.arange(KWIN) % GRID_W
    win_start = np.clip(rq - WIN_W // 2, 0, GRID_W - WIN_W)
    valid = (kc[None, :] >= win_start[:, None]) & (kc[None, :] < win_start[:, None] + WIN_W)
    hmask = (np.arange(2 * DH)[:, None] // DH) == (np.arange(2 * DH)[None, :] // DH)
    return valid.astype(np.float32), hmask.astype(np.float32)


def _toeplitz_table():
    q = np.arange(GRID_W)[:, None]
    kc = np.arange(GRID_W)[None, :]
    dc = (kc - q + WIN_W - 1).reshape(-1)
    e = np.zeros((128, GRID_W * GRID_W), np.float32)
    ok = (dc >= 0) & (dc < N_RPB_C)
    e[dc[ok], np.arange(GRID_W * GRID_W)[ok]] = 1.0
    return e


def _row_shift_table():
    s = np.zeros((16, WIN_H * WIN_H), np.float32)
    for oi in range(WIN_H):
        for i in range(WIN_H):
            s[i - oi + WIN_H - 1, oi * WIN_H + i] = 1.0
    return s


def _sigmoid(x):
    return 1.0 / (1.0 + jnp.exp(-x))


def _softplus(x):
    return jnp.maximum(x, 0.0) + jnp.log(1.0 + jnp.exp(-jnp.abs(x)))


def _expm1(x):
    series = x * (1.0 + x * (0.5 + x * (1.0 / 6.0 + x * (1.0 / 24.0))))
    return jnp.where(jnp.abs(x) < 0.02, series, jnp.exp(x) - 1.0)


_GELU_C = math.sqrt(2.0 / math.pi)


def _gelu_and_grad(x):
    x2 = x * x
    inner = _GELU_C * (x + 0.044715 * x * x2)
    t = jnp.tanh(inner)
    g = 0.5 * x * (1.0 + t)
    dg = 0.5 * (1.0 + t) + 0.5 * x * (1.0 - t * t) * _GELU_C * (1.0 + 3.0 * 0.044715 * x2)
    return g, dg


def _dot(a, b):
    return jnp.dot(a, b, preferred_element_type=F32)


def _dot_nt(a, b):
    return lax.dot_general(a, b, (((1,), (1,)), ((), ())), preferred_element_type=F32)


def _dot_tn(a, b):
    return lax.dot_general(a, b, (((0,), (0,)), ((), ())), preferred_element_type=F32)


def _shift_rows(x, s):
    n = x.shape[0]
    rows = lax.broadcasted_iota(jnp.int32, x.shape, 0)
    y = pltpu.roll(x, s % n, 0)
    if s > 0:
        return jnp.where(rows >= s, y, 0.0)
    return jnp.where(rows < n + s, y, 0.0)


def _matmul(a, b, mode, out_dtype, name, tm=512, tn=512, tk=512):
    if mode == "nn":
        (m, k), (k2, n) = a.shape, b.shape
    elif mode == "nt":
        (m, k), (n, k2) = a.shape, b.shape
    else:
        (k, m), (k2, n) = a.shape, b.shape
    assert k == k2
    tm, tn, tk = min(tm, m), min(tn, n), min(tk, k)
    assert m % tm == 0 and n % tn == 0 and k % tk == 0
    nk = k // tk
    dot = {"nn": _dot, "nt": _dot_nt, "tn": _dot_tn}[mode]

    def body(a_ref, b_ref, o_ref, acc):
        kk = pl.program_id(2)

        @pl.when(kk == 0)
        def _():
            acc[...] = jnp.zeros_like(acc)

        acc[...] += dot(a_ref[...].astype(BF16), b_ref[...].astype(BF16))

        @pl.when(kk == nk - 1)
        def _():
            o_ref[...] = acc[...].astype(out_dtype)

    if mode == "tn":
        a_spec = pl.BlockSpec((tk, tm), lambda i, j, kk: (kk, i))
    else:
        a_spec = pl.BlockSpec((tm, tk), lambda i, j, kk: (i, kk))
    if mode == "nt":
        b_spec = pl.BlockSpec((tn, tk), lambda i, j, kk: (j, kk))
    else:
        b_spec = pl.BlockSpec((tk, tn), lambda i, j, kk: (kk, j))
    return pl.pallas_call(
        body, name=name,
        out_shape=jax.ShapeDtypeStruct((m, n), out_dtype),
        grid=(m // tm, n // tn, nk),
        in_specs=[a_spec, b_spec],
        out_specs=pl.BlockSpec((tm, tn), lambda i, j, kk: (i, j)),
        scratch_shapes=[pltpu.VMEM((tm, tn), F32)],
        compiler_params=_params(dimension_semantics=("parallel", "parallel", "arbitrary")),
    )(a, b)


def _in_proj(x, g1, w_in, b_in):
    tm, tn = 512, 512
    nj = D_IN // tn

    def body(x_ref, g_ref, w_ref, b_ref, qkv_ref, uy_ref, gg_ref, h_ref, h_scr):
        j = pl.program_id(1)

        @pl.when(j == 0)
        def _():
            xv = x_ref[...]
            r = lax.rsqrt(jnp.mean(xv * xv, axis=-1, keepdims=True) + EPS)
            h = ((xv * r) * g_ref[...]).astype(BF16)
            h_scr[...] = h
            h_ref[...] = h

        z = _dot(h_scr[...], w_ref[...]) + b_ref[...]

        @pl.when(j < 3)
        def _():
            qkv_ref[...] = z.astype(BF16)

        @pl.when((j >= 3) & (j < 7))
        def _():
            uy_ref[...] = z

        @pl.when(j >= 7)
        def _():
            gg_ref[...] = z

    return pl.pallas_call(
        body, name="in_proj",
        out_shape=(jax.ShapeDtypeStruct((T, 3 * D_ATT), BF16),
                   jax.ShapeDtypeStruct((T, 2 * D_REC), F32),
                   jax.ShapeDtypeStruct((T, 2 * D), F32),
                   jax.ShapeDtypeStruct((T, D), BF16)),
        grid=(T // tm, nj),
        in_specs=[pl.BlockSpec((tm, D), lambda i, j: (i, 0)),
                  pl.BlockSpec((1, D), lambda i, j: (0, 0)),
                  pl.BlockSpec((D, tn), lambda i, j: (0, j)),
                  pl.BlockSpec((1, tn), lambda i, j: (0, j))],
        out_specs=(pl.BlockSpec((tm, tn), lambda i, j: (i, jnp.minimum(j, 2))),
                   pl.BlockSpec((tm, tn), lambda i, j: (i, jnp.clip(j - 3, 0, 3))),
                   pl.BlockSpec((tm, tn), lambda i, j: (i, jnp.clip(j - 7, 0, 3))),
                   pl.BlockSpec((tm, D), lambda i, j: (i, 0))),
        scratch_shapes=[pltpu.VMEM((tm, D), BF16)],
        compiler_params=_params(dimension_semantics=("parallel", "arbitrary")),
    )(x, g1, w_in, b_in)


def _rpb_expand(rpb):
    rpb2 = jnp.pad(rpb.reshape(N_HEADS * N_RPB_R, N_RPB_C), ((0, 0), (0, 128 - N_RPB_C)))
    table = jnp.asarray(_toeplitz_table())

    def body(r_ref, e_ref, o_ref):
        o_ref[...] = jnp.dot(r_ref[...], e_ref[...], precision=lax.Precision.HIGHEST,
                             preferred_element_type=F32)

    tb = pl.pallas_call(
        body, name="rpb_expand",
        out_shape=jax.ShapeDtypeStruct((N_HEADS * N_RPB_R, GRID_W * GRID_W), F32),
    )(rpb2, table)
    tb = tb.reshape(N_HEADS, N_RPB_R, GRID_W, GRID_W)
    variants = []
    for oi in range(WIN_H):
        sl = tb[:, WIN_H - 1 - oi: 2 * WIN_H - 1 - oi]
        sl = sl.transpose(0, 2, 1, 3).reshape(N_HEADS // 2, 2 * GRID_W, KWIN)
        variants.append(sl)
    return jnp.stack(variants, axis=0)


def _rpb_reduce(gbias):
    g = gbias.reshape(N_HEADS // 2, WIN_H, 2, GRID_W, WIN_H, GRID_W)
    g = g.transpose(0, 2, 1, 4, 3, 5).reshape(N_HEADS, WIN_H * WIN_H, GRID_W * GRID_W)
    table = jnp.asarray(_toeplitz_table().T.copy())
    shift = jnp.asarray(_row_shift_table())

    def body(g_ref, e_ref, s_ref, o_ref):
        r = jnp.dot(g_ref[0], e_ref[...], precision=lax.Precision.HIGHEST,
                    preferred_element_type=F32)
        o_ref[0] = jnp.dot(s_ref[...], r, precision=lax.Precision.HIGHEST,
                           preferred_element_type=F32)

    out = pl.pallas_call(
        body, name="rpb_reduce",
        out_shape=jax.ShapeDtypeStruct((N_HEADS, 16, 128), F32),
        grid=(N_HEADS,),
        in_specs=[pl.BlockSpec((1, WIN_H * WIN_H, GRID_W * GRID_W), lambda h: (h, 0, 0)),
                  pl.BlockSpec((GRID_W * GRID_W, 128), lambda h: (0, 0)),
                  pl.BlockSpec((16, WIN_H * WIN_H), lambda h: (0, 0))],
        out_specs=pl.BlockSpec((1, 16, 128), lambda h: (h, 0, 0)),
        compiler_params=_params(dimension_semantics=("arbitrary",)),
    )(g, table, shift)
    return out[:, :N_RPB_R, :N_RPB_C]


def _att_scores(q_ref, k_ref, bias_ref, valid, hmask, r):
    rs = jnp.clip(r - WIN_H // 2, 0, ROWS - WIN_H)
    oi = r - rs
    q0 = pl.multiple_of(r * GRID_W, GRID_W)
    k0 = pl.multiple_of(rs * GRID_W, GRID_W)
    q_r = q_ref[pl.ds(q0, GRID_W), :]
    q2 = jnp.where(hmask, jnp.concatenate([q_r, q_r], axis=0), jnp.zeros((), BF16))
    kw = k_ref[pl.ds(k0, KWIN), :]
    s = _dot_nt(q2, kw) * (DH ** -0.5) + bias_ref[oi, 0]
    s = jnp.where(valid, s, -1e30)
    m = jnp.max(s, axis=-1, keepdims=True)
    p = jnp.exp(s - m)
    p = p / jnp.sum(p, axis=-1, keepdims=True)
    return p, q2, kw, q0, k0, oi


def _att_fwd(qkv, bias):
    valid_np, hmask_np = _att_tables()

    def body(q_ref, k_ref, v_ref, bias_ref, valid_ref, hmask_ref, o_ref):
        valid = valid_ref[...] > 0.5
        hmask = hmask_ref[...] > 0.5
        first_head = lax.broadcasted_iota(jnp.int32, (GRID_W, 2 * DH), 1) < DH

        def row(r, carry):
            p, _, _, q0, k0, _ = _att_scores(q_ref, k_ref, bias_ref, valid, hmask, r)
            o2 = _dot(p.astype(BF16), v_ref[pl.ds(k0, KWIN), :])
            o_ref[pl.ds(q0, GRID_W), :] = jnp.where(first_head, o2[:GRID_W], o2[GRID_W:]).astype(BF16)
            return carry

        lax.fori_loop(0, ROWS, row, 0)

    col = lambda off: pl.BlockSpec((T, 2 * DH), lambda hp: (0, hp + off))
    return pl.pallas_call(
        body, name="att_fwd",
        out_shape=jax.ShapeDtypeStruct((T, D_ATT), BF16),
        grid=(N_HEADS // 2,),
        in_specs=[col(0), col(4), col(8),
                  pl.BlockSpec((WIN_H, 1, 2 * GRID_W, KWIN), lambda hp: (0, hp, 0, 0)),
                  pl.BlockSpec((2 * GRID_W, KWIN), lambda hp: (0, 0)),
                  pl.BlockSpec((2 * DH, 2 * DH), lambda hp: (0, 0))],
        out_specs=pl.BlockSpec((T, 2 * DH), lambda hp: (0, hp)),
        compiler_params=_params(dimension_semantics=("parallel",)),
    )(qkv, qkv, qkv, bias, jnp.asarray(valid_np), jnp.asarray(hmask_np))


def _att_bwd(qkv, bias, datt):
    valid_np, hmask_np = _att_tables()

    def body(q_ref, k_ref, v_ref, do_ref, bias_ref, valid_ref, hmask_ref,
             dq_ref, dk_ref, dv_ref, gb_ref, dk_acc, dv_acc):
        valid = valid_ref[...] > 0.5
        hmask = hmask_ref[...] > 0.5
        first_head = lax.broadcasted_iota(jnp.int32, (GRID_W, 2 * DH), 1) < DH
        dk_acc[...] = jnp.zeros_like(dk_acc)
        dv_acc[...] = jnp.zeros_like(dv_acc)
        gb_ref[...] = jnp.zeros_like(gb_ref)

        def row(r, carry):
            p, q2, kw, q0, k0, oi = _att_scores(q_ref, k_ref, bias_ref, valid, hmask, r)
            do_r = do_ref[pl.ds(q0, GRID_W), :]
            do2 = jnp.where(hmask, jnp.concatenate([do_r, do_r], axis=0), jnp.zeros((), BF16))
            vw = v_ref[pl.ds(k0, KWIN), :]
            dp = _dot_nt(do2, vw)
            ds = p * (dp - jnp.sum(dp * p, axis=-1, keepdims=True))
            p16 = p.astype(BF16)
            ds16 = ds.astype(BF16)
            dv_acc[pl.ds(k0, KWIN), :] += _dot_tn(p16, do2)
            dk_acc[pl.ds(k0, KWIN), :] += _dot_tn(ds16, q2) * (DH ** -0.5)
            dq2 = _dot(ds16, kw) * (DH ** -0.5)
            dq_ref[pl.ds(q0, GRID_W), :] = jnp.where(first_head, dq2[:GRID_W], dq2[GRID_W:]).astype(BF16)
            gb_ref[0, oi] += ds
            return carry

        lax.fori_loop(0, ROWS, row, 0)
        dk_ref[...] = dk_acc[...].astype(BF16)
        dv_ref[...] = dv_acc[...].astype(BF16)

    col = lambda off: pl.BlockSpec((T, 2 * DH), lambda hp: (0, hp + off))
    out_col = pl.BlockSpec((T, 2 * DH), lambda hp: (0, hp))
    return pl.pallas_call(
        body, name="att_bwd",
        out_shape=(jax.ShapeDtypeStruct((T, D_ATT), BF16),) * 3
        + (jax.ShapeDtypeStruct((N_HEADS // 2, WIN_H, 2 * GRID_W, KWIN), F32),),
        grid=(N_HEADS // 2,),
        in_specs=[col(0), col(4), col(8), col(0),
                  pl.BlockSpec((WIN_H, 1, 2 * GRID_W, KWIN), lambda hp: (0, hp, 0, 0)),
                  pl.BlockSpec((2 * GRID_W, KWIN), lambda hp: (0, 0)),
                  pl.BlockSpec((2 * DH, 2 * DH), lambda hp: (0, 0))],
        out_specs=(out_col, out_col, out_col,
                   pl.BlockSpec((1, WIN_H, 2 * GRID_W, KWIN), lambda hp: (hp, 0, 0, 0))),
        scratch_shapes=[pltpu.VMEM((T, 2 * DH), F32), pltpu.VMEM((T, 2 * DH), F32)],
        compiler_params=_params(dimension_semantics=("parallel",)),
    )(qkv, qkv, qkv, datt, bias, jnp.asarray(valid_np), jnp.asarray(hmask_np))


def _conv_taps(up):
    return (_shift_rows(up, 2), _shift_rows(up, 1), up, _shift_rows(up, -1))


def _gates(u, u16, wa, ba, wi, bi, lam):
    r = _sigmoid(_dot(u16, wa) + ba)
    ig = _sigmoid(_dot(u16, wi) + bi)
    sp = _softplus(-lam)
    log_a = (-LRU_C) * r * sp
    a = jnp.exp(log_a)
    mult = jnp.sqrt(jnp.maximum(-_expm1(2.0 * log_a), 0.0))
    return r, ig, sp, a, mult


def _block_scan(a_ref, b_ref, h_ref, reverse):
    c = a_ref.shape[1]
    nblk = T // 8
    rows = lax.broadcasted_iota(jnp.int32, (8, c), 0)

    def step(i, h_prev):
        blk = (nblk - 1 - i) if reverse else i
        t0 = pl.multiple_of(blk * 8, 8)
        a = a_ref[pl.ds(t0, 8), :]
        b = b_ref[pl.ds(t0, 8), :]
        for s in (1, 2, 4):
            if reverse:
                keep = rows < 8 - s
                a_s = jnp.where(keep, pltpu.roll(a, 8 - s, 0), 1.0)
                b_s = jnp.where(keep, pltpu.roll(b, 8 - s, 0), 0.0)
            else:
                keep = rows >= s
                a_s = jnp.where(keep, pltpu.roll(a, s, 0), 1.0)
                b_s = jnp.where(keep, pltpu.roll(b, s, 0), 0.0)
            b = a * b_s + b
            a = a * a_s
        h = a * h_prev + b
        h_ref[pl.ds(t0, 8), :] = h
        edge = h[0:1] if reverse else h[7:8]
        return jnp.broadcast_to(edge, (8, c))

    lax.fori_loop(0, nblk, step, jnp.zeros((8, c), F32))


def _rec_specs():
    tok = lambda off: pl.BlockSpec((T, CG), lambda g: (0, g + off))
    per_ch = lambda rows: pl.BlockSpec((rows, CG), lambda g: (0, g))
    wspec = pl.BlockSpec((2, 1, CG, CG), lambda g: (0, g, 0, 0))
    return tok, per_ch, wspec


def _rec_fwd(uy, conv_w, conv_b, wa_bd, b_a, wi_bd, b_i, lam):
    tok, per_ch, wspec = _rec_specs()

    def body(up_ref, yb_ref, cw_ref, cb_ref, wa_ref, ba_ref, wi_ref, bi_ref, lam_ref,
             hf_ref, hb_ref, yrec_ref, a_f, bx_f, a_b, bx_b):
        taps = _conv_taps(up_ref[...])
        u = cb_ref[...]
        for j in range(4):
            u = u + taps[j] * cw_ref[j:j + 1, :]
        u16 = u.astype(BF16)
        for d, (a_s, bx_s) in enumerate(((a_f, bx_f), (a_b, bx_b))):
            _, ig, _, a, mult = _gates(u, u16, wa_ref[d, 0], ba_ref[d:d + 1, :],
                                       wi_ref[d, 0], bi_ref[d:d + 1, :], lam_ref[d:d + 1, :])
            a_s[...] = a
            bx_s[...] = mult * (ig * u)
        _block_scan(a_f, bx_f, hf_ref, False)
        _block_scan(a_b, bx_b, hb_ref, True)
        gelu, _ = _gelu_and_grad(yb_ref[...])
        yrec_ref[...] = ((hf_ref[...] + hb_ref[...]) * gelu).astype(BF16)

    return pl.pallas_call(
        body, name="rec_fwd",
        out_shape=(jax.ShapeDtypeStruct((T, D_REC), F32), jax.ShapeDtypeStruct((T, D_REC), F32),
                   jax.ShapeDtypeStruct((T, D_REC), BF16)),
        grid=(N_CG,),
        in_specs=[tok(0), tok(N_CG), per_ch(4), per_ch(1), wspec, per_ch(2), wspec, per_ch(2), per_ch(2)],
        out_specs=(tok(0), tok(0), tok(0)),
        scratch_shapes=[pltpu.VMEM((T, CG), F32)] * 4,
        compiler_params=_params(dimension_semantics=("parallel",)),
    )(uy, uy, conv_w, conv_b, wa_bd, b_a, wi_bd, b_i, lam)


def _rec_bwd(uy, hf, hb, dyrec, conv_w, conv_b, wa_bd, b_a, wi_bd, b_i, lam):
    tok, per_ch, wspec = _rec_specs()

    def body(up_ref, yb_ref, hf_ref, hb_ref, dy_ref, cw_ref, cb_ref, wa_ref, ba_ref, wi_ref, bi_ref,
             lam_ref, dup_ref, dyb_ref, dcw_ref, dcb_ref, dwa_ref, dba_ref, dwi_ref, dbi_ref,
             dlam_ref, a_s, dh_s, g_s):
        taps = _conv_taps(up_ref[...])
        u = cb_ref[...]
        for j in range(4):
            u = u + taps[j] * cw_ref[j:j + 1, :]
        u16 = u.astype(BF16)
        gelu, dgelu = _gelu_and_grad(yb_ref[...])
        dy = dy_ref[...]
        dyb_ref[...] = (dy * (hf_ref[...] + hb_ref[...]) * dgelu).astype(BF16)
        dh_s[...] = dy * gelu
        du = jnp.zeros((T, CG), F32)
        for d in range(2):
            reverse = d == 1
            wa, wi = wa_ref[d, 0], wi_ref[d, 0]
            lam_d = lam_ref[d:d + 1, :]
            r, ig, sp, a, mult = _gates(u, u16, wa, ba_ref[d:d + 1, :], wi, bi_ref[d:d + 1, :], lam_d)
            a_s[...] = _shift_rows(a, 1 if reverse else -1)
            _block_scan(a_s, dh_s, g_s, not reverse)
            g = g_s[...]
            h_prev = _shift_rows(hb_ref[...], -1) if reverse else _shift_rows(hf_ref[...], 1)
            da = g * h_prev
            iu = ig * u
            dmult = g * iu
            dig = g * mult * u
            du = du + g * mult * ig
            dmult_dlog = jnp.where(mult > 0.0, -(a * a) / mult, 0.0)
            dlog_a = da * a + dmult * dmult_dlog
            dr = dlog_a * ((-LRU_C) * sp)
            dsp = jnp.sum(dlog_a * ((-LRU_C) * r), axis=0, keepdims=True)
            dlam_ref[d:d + 1, :] = dsp * (-_sigmoid(-lam_d))
            dga = dr * r * (1.0 - r)
            dgi = dig * ig * (1.0 - ig)
            dga16 = dga.astype(BF16)
            dgi16 = dgi.astype(BF16)
            du = du + _dot_nt(dga16, wa) + _dot_nt(dgi16, wi)
            dwa_ref[d, 0] = _dot_tn(u16, dga16)
            dwi_ref[d, 0] = _dot_tn(u16, dgi16)
            dba_ref[d:d + 1, :] = jnp.sum(dga, axis=0, keepdims=True)
            dbi_ref[d:d + 1, :] = jnp.sum(dgi, axis=0, keepdims=True)
        dcb_ref[...] = jnp.sum(du, axis=0, keepdims=True)
        for j in range(4):
            dcw_ref[j:j + 1, :] = jnp.sum(du * taps[j], axis=0, keepdims=True)
        dup = (_shift_rows(du, -2) * cw_ref[0:1, :] + _shift_rows(du, -1) * cw_ref[1:2, :]
               + du * cw_ref[2:3, :] + _shift_rows(du, 1) * cw_ref[3:4, :])
        dup_ref[...] = dup.astype(BF16)

    wshape = jax.ShapeDtypeStruct((2, N_CG, CG, CG), F32)
    vec = lambda rows: jax.ShapeDtypeStruct((rows, D_REC), F32)
    return pl.pallas_call(
        body, name="rec_bwd",
        out_shape=(jax.ShapeDtypeStruct((T, D_REC), BF16), jax.ShapeDtypeStruct((T, D_REC), BF16),
                   vec(4), vec(1), wshape, vec(2), wshape, vec(2), vec(2)),
        grid=(N_CG,),
        in_specs=[tok(0), tok(N_CG), tok(0), tok(0), tok(0),
                  per_ch(4), per_ch(1), wspec, per_ch(2), wspec, per_ch(2), per_ch(2)],
        out_specs=(tok(0), tok(0), per_ch(4), per_ch(1), wspec, per_ch(2), wspec, per_ch(2), per_ch(2)),
        scratch_shapes=[pltpu.VMEM((T, CG), F32)] * 3,
        compiler_params=_params(dimension_semantics=("parallel",)),
    )(uy, uy, hf, hb, dyrec, conv_w, conv_b, wa_bd, b_a, wi_bd, b_i, lam)


def _block_diag(w):
    per = CG // REC_BLOCK
    w5 = w.reshape(2, N_CG, per, REC_BLOCK, REC_BLOCK)
    eye = jnp.eye(per, dtype=w.dtype)
    bd = w5[:, :, :, :, None, :] * eye[None, None, :, None, :, None]
    return bd.reshape(2, N_CG, CG, CG)


def _block_diag_extract(dw):
    per = CG // REC_BLOCK
    d6 = dw.reshape(2, N_CG, per, REC_BLOCK, per, REC_BLOCK)
    blocks = [d6[:, :, a, :, a, :] for a in range(per)]
    return jnp.stack(blocks, axis=2).reshape(2, N_REC_BLOCKS, REC_BLOCK, REC_BLOCK)


TM_MIX = 256


def _mix_specs():
    tok = lambda width, blk=0: pl.BlockSpec((TM_MIX, width), lambda i: (i, blk))
    full = lambda shape: pl.BlockSpec(shape, lambda i: (0, 0))
    return tok, full


def _mix_fwd(x, att, yrec, gg, w_att_o, w_rec_o, w_out):
    tok, full = _mix_specs()

    def body(x_ref, att_ref, yr_ref, ga_ref, gr_ref, wao_ref, wro_ref, wo_ref, x1_ref, mixed_ref):
        y_att = _dot(att_ref[...], wao_ref[...])
        y_rec = _dot(yr_ref[...], wro_ref[...])
        mixed = (_sigmoid(ga_ref[...]) * y_att + _sigmoid(gr_ref[...]) * y_rec).astype(BF16)
        mixed_ref[...] = mixed
        x1_ref[...] = x_ref[...] + _dot(mixed, wo_ref[...])

    return pl.pallas_call(
        body, name="mix_fwd",
        out_shape=(jax.ShapeDtypeStruct((T, D), F32), jax.ShapeDtypeStruct((T, D), BF16)),
        grid=(T // TM_MIX,),
        in_specs=[tok(D), tok(D_ATT), tok(D_REC), tok(D, 0), tok(D, 1),
                  full((D_ATT, D)), full((D_REC, D)), full((D, D))],
        out_specs=(tok(D), tok(D)),
        compiler_params=_params(dimension_semantics=("parallel",)),
    )(x, att, yrec, gg, gg, w_att_o, w_rec_o, w_out)


def _mix_bwd(dx1, att, yrec, gg, w_att_o, w_rec_o, w_out):
    tok, full = _mix_specs()

    def body(dx_ref, att_ref, yr_ref, ga_ref, gr_ref, wao_ref, wro_ref, wo_ref,
             dga_ref, dgr_ref, dya_ref, dyr_ref, datt_ref, dyrp_ref):
        dmixed = _dot_nt(dx_ref[...].astype(BF16), wo_ref[...])
        y_att = _dot(att_ref[...], wao_ref[...])
        y_rec = _dot(yr_ref[...], wro_ref[...])
        sa = _sigmoid(ga_ref[...])
        sr = _sigmoid(gr_ref[...])
        dga_ref[...] = (dmixed * y_att * sa * (1.0 - sa)).astype(BF16)
        dgr_ref[...] = (dmixed * y_rec * sr * (1.0 - sr)).astype(BF16)
        dya = (dmixed * sa).astype(BF16)
        dyr = (dmixed * sr).astype(BF16)
        dya_ref[...] = dya
        dyr_ref[...] = dyr
        datt_ref[...] = _dot_nt(dya, wao_ref[...]).astype(BF16)
        dyrp_ref[...] = _dot_nt(dyr, wro_ref[...])

    return pl.pallas_call(
        body, name="mix_bwd",
        out_shape=(jax.ShapeDtypeStruct((T, D), BF16), jax.ShapeDtypeStruct((T, D), BF16),
                   jax.ShapeDtypeStruct((T, D), BF16), jax.ShapeDtypeStruct((T, D), BF16),
                   jax.ShapeDtypeStruct((T, D_ATT), BF16), jax.ShapeDtypeStruct((T, D_REC), F32)),
        grid=(T // TM_MIX,),
        in_specs=[tok(D), tok(D_ATT), tok(D_REC), tok(D, 0), tok(D, 1),
                  full((D_ATT, D)), full((D_REC, D)), full((D, D))],
        out_specs=(tok(D), tok(D), tok(D), tok(D), tok(D_ATT), tok(D_REC)),
        compiler_params=_params(dimension_semantics=("parallel",)),
    )(dx1, att, yrec, gg, gg, w_att_o, w_rec_o, w_out)


TM_FFN = 256
FF_CHUNK = 1024


def _ffn_loss(x1, target, g2, gf, w_ff1, w_ff2):
    n_chunks = D_FF // FF_CHUNK

    def body(x1_ref, tg_ref, g2_ref, gf_ref, w1_hbm, w2_hbm,
             loss_ref, dx1_ref, h2_ref, act_ref, dpre_ref, dx2_ref, dg2_ref, dgf_ref,
             w1, w2, relu_s):
        i = pl.program_id(0)

        @pl.when(i == 0)
        def _():
            pltpu.sync_copy(w1_hbm, w1)
            pltpu.sync_copy(w2_hbm, w2)
            loss_ref[...] = jnp.zeros_like(loss_ref)
            dg2_ref[...] = jnp.zeros_like(dg2_ref)
            dgf_ref[...] = jnp.zeros_like(dgf_ref)

        x1v = x1_ref[...]
        r2 = lax.rsqrt(jnp.mean(x1v * x1v, axis=-1, keepdims=True) + EPS)
        xh2 = x1v * r2
        h2 = (xh2 * g2_ref[...]).astype(BF16)
        h2_ref[...] = h2
        x2 = x1v
        for c in range(n_chunks):
            cols = slice(c * FF_CHUNK, (c + 1) * FF_CHUNK)
            rl = jnp.maximum(_dot(h2, w1[:, cols]), 0.0)
            relu_s[:, cols] = rl
            act = (rl * rl).astype(BF16)
            act_ref[:, cols] = act
            x2 = x2 + _dot(act, w2[cols, :])
        r3 = lax.rsqrt(jnp.mean(x2 * x2, axis=-1, keepdims=True) + EPS)
        xh3 = x2 * r3
        err = xh3 * gf_ref[...] - tg_ref[...]
        loss_ref[...] += 0.5 * jnp.sum(jnp.mean(err * err, axis=-1, keepdims=True))
        dy = err * (1.0 / D)
        dgf_ref[...] += jnp.sum(dy * xh3, axis=0, keepdims=True)
        dxh3 = dy * gf_ref[...]
        dx2 = r3 * (dxh3 - xh3 * jnp.mean(dxh3 * xh3, axis=-1, keepdims=True))
        dx2_16 = dx2.astype(BF16)
        dx2_ref[...] = dx2_16
        dh2 = jnp.zeros((TM_FFN, D), F32)
        for c in range(n_chunks):
            cols = slice(c * FF_CHUNK, (c + 1) * FF_CHUNK)
            dpre = (_dot_nt(dx2_16, w2[cols, :]) * (2.0 * relu_s[:, cols])).astype(BF16)
            dpre_ref[:, cols] = dpre
            dh2 = dh2 + _dot_nt(dpre, w1[:, cols])
        dg2_ref[...] += jnp.sum(dh2 * xh2, axis=0, keepdims=True)
        dxh2 = dh2 * g2_ref[...]
        dx1_ref[...] = dx2 + r2 * (dxh2 - xh2 * jnp.mean(dxh2 * xh2, axis=-1, keepdims=True))

    tok = lambda width: pl.BlockSpec((TM_FFN, width), lambda i: (i, 0))
    vec = pl.BlockSpec((1, D), lambda i: (0, 0))
    hbm = pl.BlockSpec(memory_space=pl.ANY)
    return pl.pallas_call(
        body, name="ffn_loss",
        out_shape=(jax.ShapeDtypeStruct((8, 128), F32), jax.ShapeDtypeStruct((T, D), F32),
                   jax.ShapeDtypeStruct((T, D), BF16), jax.ShapeDtypeStruct((T, D_FF), BF16),
                   jax.ShapeDtypeStruct((T, D_FF), BF16), jax.ShapeDtypeStruct((T, D), BF16),
                   jax.ShapeDtypeStruct((1, D), F32), jax.ShapeDtypeStruct((1, D), F32)),
        grid=(T // TM_FFN,),
        in_specs=[tok(D), tok(D), vec, vec, hbm, hbm],
        out_specs=(pl.BlockSpec((8, 128), lambda i: (0, 0)), tok(D), tok(D), tok(D_FF), tok(D_FF), tok(D),
                   vec, vec),
        scratch_shapes=[pltpu.VMEM((D, D_FF), BF16), pltpu.VMEM((D_FF, D), BF16),
                        pltpu.VMEM((TM_FFN, D_FF), F32)],
        compiler_params=_params(dimension_semantics=("arbitrary",)),
    )(x1, target, g2, gf, w_ff1, w_ff2)


def _norm1_bwd(x, g1, dh, dx1):
    tm = 512

    def body(x_ref, g_ref, dh_ref, dx1_ref, gx_ref, dg_ref):
        @pl.when(pl.program_id(0) == 0)
        def _():
            dg_ref[...] = jnp.zeros_like(dg_ref)

        xv = x_ref[...]
        r = lax.rsqrt(jnp.mean(xv * xv, axis=-1, keepdims=True) + EPS)
        xh = xv * r
        dhv = dh_ref[...]
        dg_ref[...] += jnp.sum(dhv * xh, axis=0, keepdims=True)
        dxh = dhv * g_ref[...]
        gx_ref[...] = dx1_ref[...] + r * (dxh - xh * jnp.mean(dxh * xh, axis=-1, keepdims=True))

    tok = pl.BlockSpec((tm, D), lambda i: (i, 0))
    vec = pl.BlockSpec((1, D), lambda i: (0, 0))
    return pl.pallas_call(
        body, name="norm1_bwd",
        out_shape=(jax.ShapeDtypeStruct((T, D), F32), jax.ShapeDtypeStruct((1, D), F32)),
        grid=(T // tm,),
        in_specs=[tok, vec, tok, tok],
        out_specs=(tok, vec),
        compiler_params=_params(dimension_semantics=("arbitrary",)),
    )(x, g1, dh, dx1)


def _col_sum(a, name):
    n = a.shape[1]
    tm = 512

    def body(a_ref, o_ref):
        @pl.when(pl.program_id(0) == 0)
        def _():
            o_ref[...] = jnp.zeros_like(o_ref)

        o_ref[...] += jnp.sum(a_ref[...].astype(F32), axis=0, keepdims=True)

    return pl.pallas_call(
        body, name=name,
        out_shape=jax.ShapeDtypeStruct((1, n), F32),
        grid=(T // tm,),
        in_specs=[pl.BlockSpec((tm, n), lambda i: (i, 0))],
        out_specs=pl.BlockSpec((1, n), lambda i: (0, 0)),
        compiler_params=_params(dimension_semantics=("arbitrary",)),
    )(a)


def _local_step(x, target, p):
    bias = _rpb_expand(p["rpb"])
    wa_bd = _block_diag(p["w_rg_a"]).astype(BF16)
    wi_bd = _block_diag(p["w_rg_i"]).astype(BF16)

    qkv, uy, gg, h = _in_proj(x, p["ln1_g"], p["w_in"], p["b_in"])
    att = _att_fwd(qkv, bias)
    hf, hb, yrec = _rec_fwd(uy, p["conv_w"], p["conv_b"], wa_bd, p["b_rg_a"], wi_bd, p["b_rg_i"],
                            p["lru_lambda"])
    x1, mixed = _mix_fwd(x, att, yrec, gg, p["w_att_o"], p["w_rec_o"], p["w_out"])
    loss8, dx1, h2, act, dpre, dx2, g_ln2, g_lnf = _ffn_loss(
        x1, target, p["ln2_g"], p["lnf_g"], p["w_ff1"], p["w_ff2"])

    dgg_a, dgg_r, dya, dyr, datt, dyrp = _mix_bwd(dx1, att, yrec, gg, p["w_att_o"], p["w_rec_o"], p["w_out"])
    dup, dyb, g_cw, g_cb, g_wa_bd, g_ba, g_wi_bd, g_bi, g_lam = _rec_bwd(
        uy, hf, hb, dyrp, p["conv_w"], p["conv_b"], wa_bd, p["b_rg_a"], wi_bd, p["b_rg_i"],
        p["lru_lambda"])
    dq, dk, dv, gbias = _att_bwd(qkv, bias, datt)
    dz = jnp.concatenate([dq, dk, dv, dup, dyb, dgg_a, dgg_r], axis=1)
    dh = _matmul(dz, p["w_in"], "nt", F32, "d_h")
    grad_x, g_ln1 = _norm1_bwd(x, p["ln1_g"], dh, dx1)

    grads = {
        "ln1_g": g_ln1,
        "w_in": _matmul(h, dz, "tn", BF16, "g_w_in"),
        "b_in": _col_sum(dz, "g_b_in"),
        "rpb": _rpb_reduce(gbias),
        "w_att_o": _matmul(att, dya, "tn", BF16, "g_w_att_o"),
        "conv_w": g_cw,
        "conv_b": g_cb,
        "w_rg_a": _block_diag_extract(g_wa_bd),
        "b_rg_a": g_ba,
        "w_rg_i": _block_diag_extract(g_wi_bd),
        "b_rg_i": g_bi,
        "lru_lambda": g_lam,
        "w_rec_o": _matmul(yrec, dyr, "tn", BF16, "g_w_rec_o"),
        "w_out": _matmul(mixed, dx1, "tn", BF16, "g_w_out"),
        "ln2_g": g_ln2,
        "w_ff1": _matmul(h2, dpre, "tn", BF16, "g_w_ff1"),
        "w_ff2": _matmul(act, dx2, "tn", BF16, "g_w_ff2"),
        "lnf_g": g_lnf,
    }
    return loss8[0, 0], grad_x, grads


MESH_ID = pl.DeviceIdType.MESH
ANY = pl.BlockSpec(memory_space=pl.ANY)


def _position():
    return lax.axis_index("x"), lax.axis_index("y"), lax.axis_index("c")


def _other_chips(x, y):
    return [(1 - x, y), (x, 1 - y), (1 - x, 1 - y)]


def _all_gather(shard, name):
    r, c_dim = shard.shape

    def body(x_ref, out_ref, send_sems, recv_sems, local_sem):
        x, y, c = _position()
        me, sibling = (x, y, c), (x, y, 1 - c)
        chips = _other_chips(x, y)

        def rows(px, py, pc):
            return out_ref.at[4 * px + 2 * py + pc]

        def copy(k, block, to, src=None):
            return pltpu.make_async_remote_copy(
                src_ref=rows(*block) if src is None else src, dst_ref=rows(*block),
                send_sem=send_sems.at[k], recv_sem=recv_sems.at[k],
                device_id=to, device_id_type=MESH_ID)

        mine = pltpu.make_async_copy(x_ref, rows(*me), local_sem)
        mine.start()
        first = [copy(0, me, sibling, src=x_ref)]
        first += [copy(1 + j, me, (*chip, c), src=x_ref) for j, chip in enumerate(chips)]
        for cp in first:
            cp.start()
        passed = [copy(4 + j, (*chip, c), sibling) for j, chip in enumerate(chips)]
        for j, chip in enumerate(chips):
            copy(1 + j, (*chip, c), me).wait_recv()
            passed[j].start()
        copy(0, sibling, me).wait_recv()
        for j, chip in enumerate(chips):
            copy(4 + j, (*chip, 1 - c), me).wait_recv()
        for cp in first + passed:
            cp.wait_send()
        mine.wait()

    return pl.pallas_call(
        body, name=name,
        out_shape=jax.ShapeDtypeStruct((N_DEV, r, c_dim), shard.dtype),
        in_specs=[ANY], out_specs=ANY,
        scratch_shapes=[pltpu.SemaphoreType.DMA((7,)), pltpu.SemaphoreType.DMA((7,)),
                        pltpu.SemaphoreType.DMA(())],
    )(shard)


def _pair_exchange(gp):
    _, r, c_dim = gp.shape

    def body(g_ref, a_ref, send_sems, recv_sems):
        x, y, c = _position()
        copies = [pltpu.make_async_remote_copy(
            src_ref=g_ref.at[2 * k + (1 - c)], dst_ref=a_ref.at[k],
            send_sem=send_sems.at[k], recv_sem=recv_sems.at[k],
            device_id=(x, y, 1 - c), device_id_type=MESH_ID) for k in range(N_CHIPS)]
        for cp in copies:
            cp.start()
        for cp in copies:
            cp.wait_recv()
        for cp in copies:
            cp.wait_send()

    return pl.pallas_call(
        body, name="grad_pair_exchange",
        out_shape=jax.ShapeDtypeStruct((N_CHIPS, r, c_dim), gp.dtype),
        in_specs=[ANY], out_specs=ANY,
        scratch_shapes=[pltpu.SemaphoreType.DMA((N_CHIPS,)), pltpu.SemaphoreType.DMA((N_CHIPS,))],
    )(gp)


ROW_TILE = 688


def _pair_add(gp, a, core):
    _, r, c_dim = gp.shape

    def body(core_ref, g_ref, a_ref, p_ref):
        p_ref[...] = (g_ref[...].astype(F32) + a_ref[...].astype(F32)).astype(BF16)

    return pl.pallas_call(
        body, name="grad_pair_add",
        out_shape=jax.ShapeDtypeStruct((N_CHIPS, r, c_dim), BF16),
        grid_spec=pltpu.PrefetchScalarGridSpec(
            num_scalar_prefetch=1, grid=(N_CHIPS, r // ROW_TILE),
            in_specs=[pl.BlockSpec((1, ROW_TILE, c_dim), lambda k, i, s: (2 * k + s[0], i, 0)),
                      pl.BlockSpec((1, ROW_TILE, c_dim), lambda k, i, s: (k, i, 0))],
            out_specs=pl.BlockSpec((1, ROW_TILE, c_dim), lambda k, i, s: (k, i, 0))),
        compiler_params=_params(dimension_semantics=("parallel", "parallel")),
    )(core, gp, a)


def _chip_exchange(p):
    _, r, c_dim = p.shape

    def body(p_ref, b_ref, send_sems, recv_sems):
        x, y, c = _position()
        copies = [pltpu.make_async_remote_copy(
            src_ref=p_ref.at[2 * cx + cy], dst_ref=b_ref.at[j],
            send_sem=send_sems.at[j], recv_sem=recv_sems.at[j],
            device_id=(cx, cy, c), device_id_type=MESH_ID)
            for j, (cx, cy) in enumerate(_other_chips(x, y))]
        for cp in copies:
            cp.start()
        for cp in copies:
            cp.wait_recv()
        for cp in copies:
            cp.wait_send()

    return pl.pallas_call(
        body, name="grad_chip_exchange",
        out_shape=jax.ShapeDtypeStruct((3, r, c_dim), p.dtype),
        in_specs=[ANY], out_specs=ANY,
        scratch_shapes=[pltpu.SemaphoreType.DMA((3,)), pltpu.SemaphoreType.DMA((3,))],
    )(p)


def _grad_finish(p, b, chip):
    _, r, c_dim = p.shape

    def body(chip_ref, p_ref, b_ref, g_ref):
        g = p_ref[0].astype(F32)
        for j in range(3):
            g = g + b_ref[j].astype(F32)
        g_ref[...] = g

    return pl.pallas_call(
        body, name="grad_finish",
        out_shape=jax.ShapeDtypeStruct((r, c_dim), F32),
        grid_spec=pltpu.PrefetchScalarGridSpec(
            num_scalar_prefetch=1, grid=(r // ROW_TILE,),
            in_specs=[pl.BlockSpec((1, ROW_TILE, c_dim), lambda i, s: (s[0], i, 0)),
                      pl.BlockSpec((3, ROW_TILE, c_dim), lambda i, s: (0, i, 0))],
            out_specs=pl.BlockSpec((ROW_TILE, c_dim), lambda i, s: (i, 0))),
        compiler_params=_params(dimension_semantics=("parallel",)),
    )(chip, p, b)


def _sum_devices(parts):
    _, r, c_dim = parts.shape
    tr = r // 2

    def body(p_ref, o_ref):
        s = p_ref[0]
        for d in range(1, N_DEV):
            s = s + p_ref[d]
        o_ref[...] = s

    return pl.pallas_call(
        body, name="small_grad_sum",
        out_shape=jax.ShapeDtypeStruct((r, c_dim), F32),
        grid=(2,),
        in_specs=[pl.BlockSpec((N_DEV, tr, c_dim), lambda i: (0, i, 0))],
        out_specs=pl.BlockSpec((tr, c_dim), lambda i: (i, 0)),
        compiler_params=_params(dimension_semantics=("parallel",)),
    )(parts)


def _adamw(w, g, m, v, name):
    rows, cols = w.shape
    tr = rows
    while tr * cols * 4 > (1 << 20) and tr % 16 == 0:
        tr //= 2
    c1 = 1.0 / (1.0 - ADAM_B1 ** ADAM_STEP)
    c2 = 1.0 / (1.0 - ADAM_B2 ** ADAM_STEP)

    def body(w_ref, g_ref, m_ref, v_ref, d_ref, nm_ref, nv_ref):
        gv = g_ref[...]
        nm = ADAM_B1 * m_ref[...] + (1.0 - ADAM_B1) * gv
        nv = ADAM_B2 * v_ref[...] + (1.0 - ADAM_B2) * (gv * gv)
        nm_ref[...] = nm
        nv_ref[...] = nv
        d_ref[...] = (-ADAM_LR) * ((nm * c1) / (jnp.sqrt(nv * c2) + ADAM_EPS) + ADAM_WD * w_ref[...])

    spec = pl.BlockSpec((tr, cols), lambda i: (i, 0))
    shape = jax.ShapeDtypeStruct((rows, cols), F32)
    return pl.pallas_call(
        body, name=name,
        out_shape=(shape, shape, shape),
        grid=(rows // tr,),
        in_specs=[spec] * 4, out_specs=(spec,) * 3,
        compiler_params=_params(dimension_semantics=("parallel",)),
    )(w, g, m, v)


COL_SHARDED = {"w_in": (D, D_IN), "w_att_o": (D_ATT, D), "w_ff1": (D, D_FF)}
ROW_SHARDED = {"w_rec_o": (D_REC, D), "w_out": (D, D), "w_ff2": (D_FF, D)}
CHAN = (("conv_w", 4), ("b_rg_a", 2), ("b_rg_i", 2), ("lru_lambda", 2))
SMALL = (("ln1_g", (1, D)), ("b_in", (1, D_IN)), ("rpb", (1, N_HEADS, N_RPB_R, N_RPB_C)),
         ("conv_b", (1, D_REC)), ("w_rg_a", (1, 2, N_REC_BLOCKS, REC_BLOCK, REC_BLOCK)),
         ("w_rg_i", (1, 2, N_REC_BLOCKS, REC_BLOCK, REC_BLOCK)), ("ln2_g", (1, D)), ("lnf_g", (D,)))
SMALL_ROWS = 2160
LANES = 128


def _pack_weight_shards(s):
    parts = [s[n].astype(BF16).reshape(-1, D) for n, _ in PACK_ROWS[:-1]]
    chan = jnp.concatenate([s[n].reshape(rows, LANES) for n, rows in CHAN], axis=0)
    bits = lax.bitcast_convert_type(chan, BF16).reshape(-1)
    parts.append(jnp.pad(bits, (0, 16 * D - bits.shape[0])).reshape(16, D))
    return jnp.concatenate(parts, axis=0)


def _unpack_gathered(g):
    out, r0 = {}, 0
    for n, rows in PACK_ROWS[:-1]:
        blk = g[:, r0:r0 + rows]
        r0 += rows
        if n in COL_SHARDED:
            k, ncols = COL_SHARDED[n]
            out[n] = blk.reshape(N_DEV, k, ncols // N_DEV).transpose(1, 0, 2).reshape(k, ncols)
        else:
            out[n] = blk.reshape(ROW_SHARDED[n])
    bits = g[:, r0:r0 + 16].reshape(N_DEV, 16 * D)[:, :2 * N_CHAN_ROWS * LANES]
    chan = lax.bitcast_convert_type(bits.reshape(N_DEV, N_CHAN_ROWS, LANES, 2), F32)
    chan = chan.transpose(1, 0, 2).reshape(N_CHAN_ROWS, D)
    r0 = 0
    for n, rows in CHAN:
        out[n] = chan[r0:r0 + rows]
        r0 += rows
    return out


def _pack_grads(grads):
    parts = []
    for n, _ in PACK_ROWS[:-1]:
        gw = grads[n].astype(BF16)
        if n in COL_SHARDED:
            k, ncols = COL_SHARDED[n]
            gw = gw.reshape(k, N_DEV, ncols // N_DEV).transpose(1, 0, 2)
        parts.append(gw.reshape(N_DEV, -1, D))
    chan = jnp.concatenate([grads[n] for n, _ in CHAN], axis=0)
    chan = chan.reshape(N_CHAN_ROWS, N_DEV, LANES).transpose(1, 0, 2).astype(BF16).reshape(N_DEV, -1)
    parts.append(jnp.pad(chan, ((0, 0), (0, 16 * D - chan.shape[1]))).reshape(N_DEV, 16, D))
    return jnp.concatenate(parts, axis=1)


def _pack_small(d):
    flat = jnp.concatenate([d[n].reshape(-1) for n, _ in SMALL])
    return jnp.pad(flat, (0, SMALL_ROWS * LANES - flat.shape[0])).reshape(SMALL_ROWS, LANES)


def _unpack_small(a):
    flat, out, o = a.reshape(-1), {}, 0
    for n, shape in SMALL:
        size = int(np.prod(shape))
        out[n] = flat[o:o + size].reshape(shape)
        o += size
    return out


NAMES = ("ln1_g", "w_in", "b_in", "rpb", "w_att_o", "conv_w", "conv_b", "w_rg_a", "b_rg_a", "w_rg_i",
         "b_rg_i", "lru_lambda", "w_rec_o", "w_out", "ln2_g", "w_ff1", "w_ff2", "lnf_g")


def kernel(x, ln1_g, w_in, b_in, rpb, w_att_o, conv_w, conv_b, w_rg_a, b_rg_a, w_rg_i, b_rg_i, lru_lambda, w_rec_o, w_out, ln2_g, w_ff1, w_ff2, lnf_g, loss_target, m_ln1_g, m_w_in, m_b_in, m_rpb, m_w_att_o, m_conv_w, m_conv_b, m_w_rg_a, m_b_rg_a, m_w_rg_i, m_b_rg_i, m_lru_lambda, m_w_rec_o, m_w_out, m_ln2_g, m_w_ff1, m_w_ff2, m_lnf_g, v_ln1_g, v_w_in, v_b_in, v_rpb, v_w_att_o, v_conv_w, v_conv_b, v_w_rg_a, v_b_rg_a, v_w_rg_i, v_b_rg_i, v_lru_lambda, v_w_rec_o, v_w_out, v_ln2_g, v_w_ff1, v_w_ff2, v_lnf_g):
    w = dict(zip(NAMES, (ln1_g, w_in, b_in, rpb, w_att_o, conv_w, conv_b, w_rg_a, b_rg_a, w_rg_i,
                         b_rg_i, lru_lambda, w_rec_o, w_out, ln2_g, w_ff1, w_ff2, lnf_g)))
    m = dict(zip(NAMES, (m_ln1_g, m_w_in, m_b_in, m_rpb, m_w_att_o, m_conv_w, m_conv_b, m_w_rg_a,
                         m_b_rg_a, m_w_rg_i, m_b_rg_i, m_lru_lambda, m_w_rec_o, m_w_out, m_ln2_g,
                         m_w_ff1, m_w_ff2, m_lnf_g)))
    v = dict(zip(NAMES, (v_ln1_g, v_w_in, v_b_in, v_rpb, v_w_att_o, v_conv_w, v_conv_b, v_w_rg_a,
                         v_b_rg_a, v_w_rg_i, v_b_rg_i, v_lru_lambda, v_w_rec_o, v_w_out, v_ln2_g,
                         v_w_ff1, v_w_ff2, v_lnf_g)))
    xi, yi, ci = _position()

    gathered = _all_gather(_pack_weight_shards({n: w[n][0] for n, _ in PACK_ROWS[:-1]}
                                               | {n: w[n][0] for n, _ in CHAN}), "weight_all_gather")
    p = _unpack_gathered(gathered)
    p.update(ln1_g=w["ln1_g"], b_in=w["b_in"], rpb=w["rpb"][0], conv_b=w["conv_b"],
             w_rg_a=w["w_rg_a"][0], w_rg_i=w["w_rg_i"][0], ln2_g=w["ln2_g"],
             lnf_g=w["lnf_g"].reshape(1, D))

    loss_part, grad_x, grads = _local_step(x[0], loss_target[0], p)
    loss = lax.psum(loss_part, MESH_AXES)

    core = jnp.reshape(ci, (1,)).astype(jnp.int32)
    chip = jnp.reshape(2 * xi + yi, (1,)).astype(jnp.int32)
    gp = _pack_grads(grads)
    partial_sum = _pair_add(gp, _pair_exchange(gp), core)
    g_packed = _grad_finish(partial_sum, _chip_exchange(partial_sum), chip)
    small_parts = _all_gather(_pack_small(grads), "small_grad_all_gather")
    g_small = _unpack_small(_sum_devices(small_parts))

    g, delta, new_m, new_v = dict(g_small), {}, {}, {}
    r0 = 0
    for n, rows in PACK_ROWS[:-1]:
        shape = w[n].shape[1:]
        g2 = g_packed[r0:r0 + rows].reshape(shape)
        r0 += rows
        d2, m2, v2 = _adamw(w[n][0], g2, m[n][0], v[n][0], "adamw_" + n)
        g[n], delta[n], new_m[n], new_v[n] = g2[None], d2[None], m2[None], v2[None]

    chan_g = g_packed[r0:r0 + 16].reshape(-1)[:N_CHAN_ROWS * LANES].reshape(N_CHAN_ROWS, LANES)
    pad16 = lambda a: jnp.pad(a, ((0, 16 - N_CHAN_ROWS), (0, 0)))
    cat = lambda d: pad16(jnp.concatenate([d[n].reshape(rows, LANES) for n, rows in CHAN], axis=0))
    chan_out = _adamw(cat(w), pad16(chan_g), cat(m), cat(v), "adamw_channel_vectors")
    r0 = 0
    for n, rows in CHAN:
        g[n] = chan_g[r0:r0 + rows].reshape(w[n].shape)
        delta[n], new_m[n], new_v[n] = (a[r0:r0 + rows].reshape(w[n].shape) for a in chan_out)
        r0 += rows

    small_out = _adamw(_pack_small(w), _pack_small(g_small), _pack_small(m), _pack_small(v), "adamw_replicated")
    for res, a in zip((delta, new_m, new_v), small_out):
        res.update(_unpack_small(a))

    return (loss, grad_x[None], *[g[n] for n in NAMES], *[delta[n] for n in NAMES],
            *[new_m[n] for n in NAMES], *[new_v[n] for n in NAMES])
```

```python
import math

import numpy as np
import jax
import jax.numpy as jnp
from jax import lax
from jax.experimental import pallas as pl
from jax.experimental.pallas import tpu as pltpu

F32 = jnp.float32
BF16 = jnp.bfloat16

T = 2048
D = 1024
D_ATT = 512
D_REC = 1024
D_FF = 4096
D_IN = 5632
N_HEADS = 8
DH = 64
GRID_W = 64
ROWS = T // GRID_W
WIN_H = 8
WIN_W = 16
KWIN = WIN_H * GRID_W
N_RPB_R = 2 * WIN_H - 1
N_RPB_C = 2 * WIN_W - 1
N_REC_BLOCKS = 16
REC_BLOCK = 64
CG = 128
N_CG = D_REC // CG
LRU_C = 8.0
EPS = 1e-6
N_DEV = 8
N_CHIPS = 4
LANES = 128

ADAM_LR = 0.001
ADAM_B1 = 0.9
ADAM_B2 = 0.999
ADAM_EPS = 1e-08
ADAM_WD = 0.01
ADAM_STEP = 10

MESH_AXES = ("x", "y", "c")
VMEM_LIMIT = 56 * 1024 * 1024

TILE = 512
DZ_SEGMENTS = ((0, 1), (1, 1), (2, 1), (3, 2), (5, 2), (7, 2), (9, 2))
N_DZ_TILES = D_IN // TILE


def _params(**kw):
    return pltpu.CompilerParams(vmem_limit_bytes=VMEM_LIMIT, **kw)


def _att_tables():
    rq = np.arange(2 * GRID_W) % GRID_W
    kc = np.arange(KWIN) % GRID_W
    win_start = np.clip(rq - WIN_W // 2, 0, GRID_W - WIN_W)
    valid = (kc[None, :] >= win_start[:, None]) & (kc[None, :] < win_start[:, None] + WIN_W)
    return valid.astype(np.float32), _pair_mask()


def _pair_mask():
    half = np.arange(2 * DH) // DH
    return (half[:, None] == half[None, :]).astype(np.float32)


def _dup_table():
    return np.concatenate([np.eye(REC_BLOCK, dtype=np.float32)] * 2, axis=1)


def _toeplitz_table():
    q = np.arange(GRID_W)[:, None]
    kc = np.arange(GRID_W)[None, :]
    dc = (kc - q + WIN_W - 1).reshape(-1)
    e = np.zeros((128, GRID_W * GRID_W), np.float32)
    ok = (dc >= 0) & (dc < N_RPB_C)
    e[dc[ok], np.arange(GRID_W * GRID_W)[ok]] = 1.0
    return e


def _row_shift_table():
    s = np.zeros((16, WIN_H * WIN_H), np.float32)
    for oi in range(WIN_H):
        for i in range(WIN_H):
            s[i - oi + WIN_H - 1, oi * WIN_H + i] = 1.0
    return s


def _sigmoid(x):
    return 1.0 / (1.0 + jnp.exp(-x))


def _softplus(x):
    return jnp.maximum(x, 0.0) + jnp.log(1.0 + jnp.exp(-jnp.abs(x)))


def _expm1(x):
    series = x * (1.0 + x * (0.5 + x * (1.0 / 6.0 + x * (1.0 / 24.0))))
    return jnp.where(jnp.abs(x) < 0.02, series, jnp.exp(x) - 1.0)


_GELU_C = math.sqrt(2.0 / math.pi)


def _gelu_and_grad(x):
    x2 = x * x
    inner = _GELU_C * (x + 0.044715 * x * x2)
    t = jnp.tanh(inner)
    g = 0.5 * x * (1.0 + t)
    dg = 0.5 * (1.0 + t) + 0.5 * x * (1.0 - t * t) * _GELU_C * (1.0 + 3.0 * 0.044715 * x2)
    return g, dg


def _dot(a, b):
    return jnp.dot(a, b, preferred_element_type=F32)


def _dot_nt(a, b):
    return lax.dot_general(a, b, (((1,), (1,)), ((), ())), preferred_element_type=F32)


def _dot_tn(a, b):
    return lax.dot_general(a, b, (((0,), (0,)), ((), ())), preferred_element_type=F32)


def _dot_exact(a, b):
    return jnp.dot(a, b, precision=lax.Precision.HIGHEST, preferred_element_type=F32)


def _shift_rows(x, s):
    n = x.shape[0]
    rows = lax.broadcasted_iota(jnp.int32, x.shape, 0)
    y = pltpu.roll(x, s % n, 0)
    if s > 0:
        return jnp.where(rows >= s, y, 0.0)
    return jnp.where(rows < n + s, y, 0.0)


def _rms_bwd(dh, xh, r, g):
    dxh = dh * g
    return r * (dxh - xh * jnp.mean(dxh * xh, axis=-1, keepdims=True))


def _matmul(a, b, mode, out_dtype, name, tm=512, tn=1024, tk=2048):
    if mode == "nn":
        (m, k), (k2, n) = a.shape, b.shape
    elif mode == "nt":
        (m, k), (n, k2) = a.shape, b.shape
    else:
        (k, m), (k2, n) = a.shape, b.shape
    assert k == k2
    tm, tn, tk = min(tm, m), min(tn, n), min(tk, k)
    assert m % tm == 0 and n % tn == 0 and k % tk == 0
    nk = k // tk
    dot = {"nn": _dot, "nt": _dot_nt, "tn": _dot_tn}[mode]

    def body(a_ref, b_ref, o_ref, acc):
        kk = pl.program_id(2)
        part = dot(a_ref[...].astype(BF16), b_ref[...].astype(BF16))
        if nk == 1:
            o_ref[...] = part.astype(out_dtype)
            return

        @pl.when(kk == 0)
        def _():
            acc[...] = part

        @pl.when(kk > 0)
        def _():
            acc[...] += part

        @pl.when(kk == nk - 1)
        def _():
            o_ref[...] = acc[...].astype(out_dtype)

    if mode == "tn":
        a_spec = pl.BlockSpec((tk, tm), lambda i, j, kk: (kk, i))
    else:
        a_spec = pl.BlockSpec((tm, tk), lambda i, j, kk: (i, kk))
    if mode == "nt":
        b_spec = pl.BlockSpec((tn, tk), lambda i, j, kk: (j, kk))
    else:
        b_spec = pl.BlockSpec((tk, tn), lambda i, j, kk: (kk, j))
    return pl.pallas_call(
        body, name=name,
        out_shape=jax.ShapeDtypeStruct((m, n), out_dtype),
        grid=(m // tm, n // tn, nk),
        in_specs=[a_spec, b_spec],
        out_specs=pl.BlockSpec((tm, tn), lambda i, j, kk: (i, j)),
        scratch_shapes=[pltpu.VMEM((tm, tn) if nk > 1 else (8, LANES), F32)],
        compiler_params=_params(dimension_semantics=("parallel", "parallel", "arbitrary")),
    )(a, b)


def _in_proj(x, g1, w_in_t, b_in):
    tm = 512

    def body(x_ref, g_ref, w_ref, b_ref, qkv_ref, uy_ref, gg_ref, h_ref, h_scr):
        j = pl.program_id(1)

        @pl.when(j == 0)
        def _():
            xv = x_ref[...]
            r = lax.rsqrt(jnp.mean(xv * xv, axis=-1, keepdims=True) + EPS)
            h = ((xv * r) * g_ref[...]).astype(BF16)
            h_scr[...] = h
            h_ref[...] = h

        z = _dot_nt(h_scr[...], w_ref[...]) + b_ref[...]

        @pl.when(j < 3)
        def _():
            qkv_ref[...] = z.astype(BF16)

        @pl.when((j >= 3) & (j < 7))
        def _():
            uy_ref[...] = z

        @pl.when(j >= 7)
        def _():
            gg_ref[...] = z

    return pl.pallas_call(
        body, name="in_proj",
        out_shape=(jax.ShapeDtypeStruct((T, 3 * D_ATT), BF16),
                   jax.ShapeDtypeStruct((T, 2 * D_REC), F32),
                   jax.ShapeDtypeStruct((T, 2 * D), F32),
                   jax.ShapeDtypeStruct((T, D), BF16)),
        grid=(T // tm, N_DZ_TILES),
        in_specs=[pl.BlockSpec((tm, D), lambda i, j: (i, 0)),
                  pl.BlockSpec((1, D), lambda i, j: (0, 0)),
                  pl.BlockSpec((TILE, D), lambda i, j: (j, 0)),
                  pl.BlockSpec((1, TILE), lambda i, j: (0, j))],
        out_specs=(pl.BlockSpec((tm, TILE), lambda i, j: (i, jnp.minimum(j, 2))),
                   pl.BlockSpec((tm, TILE), lambda i, j: (i, jnp.clip(j - 3, 0, 3))),
                   pl.BlockSpec((tm, TILE), lambda i, j: (i, jnp.clip(j - 7, 0, 3))),
                   pl.BlockSpec((tm, D), lambda i, j: (i, 0))),
        scratch_shapes=[pltpu.VMEM((tm, D), BF16)],
        compiler_params=_params(dimension_semantics=("parallel", "arbitrary")),
    )(x, g1, w_in_t, b_in)


def _segment_spec(rows, seg, row_index):
    off, n = seg
    if row_index:
        return pl.BlockSpec((rows, TILE), lambda i, j: (i, jnp.clip(j - off, 0, n - 1)))
    return pl.BlockSpec((rows, TILE), lambda j, kk: (kk, jnp.clip(j - off, 0, n - 1)))


def _dh_norm1_bwd(dz_segments, w_in_t, x, g1, dx1):
    tm = 512

    def body(*refs):
        seg_refs = refs[:7]
        w_ref, x_ref, g_ref, dx1_ref, gx_ref, dg_ref, acc = refs[7:]
        i, kk = pl.program_id(0), pl.program_id(1)

        @pl.when(kk == 0)
        def _():
            acc[...] = jnp.zeros_like(acc)

        for s, (off, n) in enumerate(DZ_SEGMENTS):
            @pl.when((kk >= off) & (kk < off + n))
            def _(s=s):
                acc[...] += _dot(seg_refs[s][...], w_ref[...])

        @pl.when((i == 0) & (kk == 0))
        def _():
            dg_ref[...] = jnp.zeros_like(dg_ref)

        @pl.when(kk == N_DZ_TILES - 1)
        def _():
            xv = x_ref[...]
            r = lax.rsqrt(jnp.mean(xv * xv, axis=-1, keepdims=True) + EPS)
            xh = xv * r
            dh = acc[...]
            dg_ref[...] += jnp.sum(dh * xh, axis=0, keepdims=True)
            gx_ref[...] = dx1_ref[...] + _rms_bwd(dh, xh, r, g_ref[...])

    tok = pl.BlockSpec((tm, D), lambda i, j: (i, 0))
    vec = pl.BlockSpec((1, D), lambda i, j: (0, 0))
    return pl.pallas_call(
        body, name="dh_norm1_bwd",
        out_shape=(jax.ShapeDtypeStruct((T, D), F32), jax.ShapeDtypeStruct((1, D), F32)),
        grid=(T // tm, N_DZ_TILES),
        in_specs=[_segment_spec(tm, seg, True) for seg in DZ_SEGMENTS]
        + [pl.BlockSpec((TILE, D), lambda i, j: (j, 0)), tok, vec, tok],
        out_specs=(tok, vec),
        scratch_shapes=[pltpu.VMEM((tm, D), F32)],
        compiler_params=_params(dimension_semantics=("arbitrary", "arbitrary")),
    )(*dz_segments, w_in_t, x, g1, dx1)


def _grad_w_in(dz_segments, h):
    tk = 1024
    nk = T // tk

    def body(*refs):
        seg_refs = refs[:7]
        h_ref, gw_ref, gb_ref, acc = refs[7:]
        j, kk = pl.program_id(0), pl.program_id(1)

        @pl.when(kk == 0)
        def _():
            acc[...] = jnp.zeros_like(acc)
            gb_ref[...] = jnp.zeros_like(gb_ref)

        for s, (off, n) in enumerate(DZ_SEGMENTS):
            @pl.when((j >= off) & (j < off + n))
            def _(s=s):
                a = seg_refs[s][...]
                acc[...] += _dot_tn(a, h_ref[...])
                gb_ref[...] += jnp.sum(a.astype(F32), axis=0, keepdims=True)

        @pl.when(kk == nk - 1)
        def _():
            gw_ref[...] = acc[...].astype(BF16)

    return pl.pallas_call(
        body, name="grad_w_in",
        out_shape=(jax.ShapeDtypeStruct((D_IN, D), BF16), jax.ShapeDtypeStruct((1, D_IN), F32)),
        grid=(N_DZ_TILES, nk),
        in_specs=[_segment_spec(tk, seg, False) for seg in DZ_SEGMENTS]
        + [pl.BlockSpec((tk, D), lambda j, kk: (kk, 0))],
        out_specs=(pl.BlockSpec((TILE, D), lambda j, kk: (j, 0)),
                   pl.BlockSpec((1, TILE), lambda j, kk: (0, j))),
        scratch_shapes=[pltpu.VMEM((TILE, D), F32)],
        compiler_params=_params(dimension_semantics=("parallel", "arbitrary")),
    )(*dz_segments, h)


def _rpb_expand(rpb):
    rpb2 = jnp.pad(rpb.reshape(N_HEADS * N_RPB_R, N_RPB_C), ((0, 0), (0, 128 - N_RPB_C)))
    table = jnp.asarray(_toeplitz_table())

    def body(r_ref, e_ref, o_ref):
        o_ref[...] = _dot_exact(r_ref[...], e_ref[...])

    tb = pl.pallas_call(
        body, name="rpb_expand",
        out_shape=jax.ShapeDtypeStruct((N_HEADS * N_RPB_R, GRID_W * GRID_W), F32),
    )(rpb2, table)
    tb = tb.reshape(N_HEADS, N_RPB_R, GRID_W, GRID_W)
    variants = []
    for oi in range(WIN_H):
        sl = tb[:, WIN_H - 1 - oi: 2 * WIN_H - 1 - oi]
        sl = sl.transpose(0, 2, 1, 3).reshape(N_HEADS // 2, 2 * GRID_W, KWIN)
        variants.append(sl)
    return jnp.stack(variants, axis=0)


def _rpb_reduce(gbias):
    g = gbias.reshape(N_HEADS // 2, WIN_H, 2, GRID_W, WIN_H, GRID_W)
    g = g.transpose(0, 2, 1, 4, 3, 5).reshape(N_HEADS, WIN_H * WIN_H, GRID_W * GRID_W)
    table = jnp.asarray(_toeplitz_table().T.copy())
    shift = jnp.asarray(_row_shift_table())

    def body(g_ref, e_ref, s_ref, o_ref):
        r = _dot_exact(g_ref[0], e_ref[...])
        o_ref[0] = _dot_exact(s_ref[...], r)

    out = pl.pallas_call(
        body, name="rpb_reduce",
        out_shape=jax.ShapeDtypeStruct((N_HEADS, 16, 128), F32),
        grid=(N_HEADS,),
        in_specs=[pl.BlockSpec((1, WIN_H * WIN_H, GRID_W * GRID_W), lambda h: (h, 0, 0)),
                  pl.BlockSpec((GRID_W * GRID_W, 128), lambda h: (0, 0)),
                  pl.BlockSpec((16, WIN_H * WIN_H), lambda h: (0, 0))],
        out_specs=pl.BlockSpec((1, 16, 128), lambda h: (h, 0, 0)),
        compiler_params=_params(dimension_semantics=("arbitrary",)),
    )(g, table, shift)
    return out[:, :N_RPB_R, :N_RPB_C]


def _att_scores(q_ref, k_ref, bias_ref, valid, hmask, r):
    rs = jnp.clip(r - WIN_H // 2, 0, ROWS - WIN_H)
    oi = r - rs
    q0 = pl.multiple_of(r * GRID_W, GRID_W)
    k0 = pl.multiple_of(rs * GRID_W, GRID_W)
    q_r = q_ref[pl.ds(q0, GRID_W), :]
    q2 = jnp.where(hmask, jnp.concatenate([q_r, q_r], axis=0), jnp.zeros((), BF16))
    kw = k_ref[pl.ds(k0, KWIN), :]
    s = _dot_nt(q2, kw) * (DH ** -0.5) + bias_ref[oi, 0]
    s = jnp.where(valid, s, -1e30)
    m = jnp.max(s, axis=-1, keepdims=True)
    p = jnp.exp(s - m)
    p = p / jnp.sum(p, axis=-1, keepdims=True)
    return p, q2, kw, q0, k0, oi


def _att_fwd(qkv, bias):
    valid_np, hmask_np = _att_tables()

    def body(q_ref, k_ref, v_ref, bias_ref, valid_ref, hmask_ref, o_ref):
        valid = valid_ref[...] > 0.5
        hmask = hmask_ref[...] > 0.5
        first_head = lax.broadcasted_iota(jnp.int32, (GRID_W, 2 * DH), 1) < DH

        def row(r, carry):
            p, _, _, q0, k0, _ = _att_scores(q_ref, k_ref, bias_ref, valid, hmask, r)
            o2 = _dot(p.astype(BF16), v_ref[pl.ds(k0, KWIN), :])
            o_ref[pl.ds(q0, GRID_W), :] = jnp.where(first_head, o2[:GRID_W], o2[GRID_W:]).astype(BF16)
            return carry

        lax.fori_loop(0, ROWS, row, 0)

    col = lambda off: pl.BlockSpec((T, 2 * DH), lambda hp: (0, hp + off))
    return pl.pallas_call(
        body, name="att_fwd",
        out_shape=jax.ShapeDtypeStruct((T, D_ATT), BF16),
        grid=(N_HEADS // 2,),
        in_specs=[col(0), col(4), col(8),
                  pl.BlockSpec((WIN_H, 1, 2 * GRID_W, KWIN), lambda hp: (0, hp, 0, 0)),
                  pl.BlockSpec((2 * GRID_W, KWIN), lambda hp: (0, 0)),
                  pl.BlockSpec((2 * DH, 2 * DH), lambda hp: (0, 0))],
        out_specs=pl.BlockSpec((T, 2 * DH), lambda hp: (0, hp)),
        compiler_params=_params(dimension_semantics=("parallel",)),
    )(qkv, qkv, qkv, bias, jnp.asarray(valid_np), jnp.asarray(hmask_np))


def _att_bwd(qkv, bias, datt):
    valid_np, hmask_np = _att_tables()

    def body(q_ref, k_ref, v_ref, do_ref, bias_ref, valid_ref, hmask_ref,
             dq_ref, dk_ref, dv_ref, gb_ref, dk_acc, dv_acc):
        valid = valid_ref[...] > 0.5
        hmask = hmask_ref[...] > 0.5
        first_head = lax.broadcasted_iota(jnp.int32, (GRID_W, 2 * DH), 1) < DH
        dk_acc[...] = jnp.zeros_like(dk_acc)
        dv_acc[...] = jnp.zeros_like(dv_acc)
        gb_ref[...] = jnp.zeros_like(gb_ref)

        def row(r, carry):
            p, q2, kw, q0, k0, oi = _att_scores(q_ref, k_ref, bias_ref, valid, hmask, r)
            do_r = do_ref[pl.ds(q0, GRID_W), :]
            do2 = jnp.where(hmask, jnp.concatenate([do_r, do_r], axis=0), jnp.zeros((), BF16))
            vw = v_ref[pl.ds(k0, KWIN), :]
            dp = _dot_nt(do2, vw)
            ds = p * (dp - jnp.sum(dp * p, axis=-1, keepdims=True))
            p16 = p.astype(BF16)
            ds16 = ds.astype(BF16)
            dv_acc[pl.ds(k0, KWIN), :] += _dot_tn(p16, do2)
            dk_acc[pl.ds(k0, KWIN), :] += _dot_tn(ds16, q2) * (DH ** -0.5)
            dq2 = _dot(ds16, kw) * (DH ** -0.5)
            dq_ref[pl.ds(q0, GRID_W), :] = jnp.where(first_head, dq2[:GRID_W], dq2[GRID_W:]).astype(BF16)
            gb_ref[0, oi] += ds
            return carry

        lax.fori_loop(0, ROWS, row, 0)
        dk_ref[...] = dk_acc[...].astype(BF16)
        dv_ref[...] = dv_acc[...].astype(BF16)

    col = lambda off: pl.BlockSpec((T, 2 * DH), lambda hp: (0, hp + off))
    out_col = pl.BlockSpec((T, 2 * DH), lambda hp: (0, hp))
    return pl.pallas_call(
        body, name="att_bwd",
        out_shape=(jax.ShapeDtypeStruct((T, D_ATT), BF16),) * 3
        + (jax.ShapeDtypeStruct((N_HEADS // 2, WIN_H, 2 * GRID_W, KWIN), F32),),
        grid=(N_HEADS // 2,),
        in_specs=[col(0), col(4), col(8), col(0),
                  pl.BlockSpec((WIN_H, 1, 2 * GRID_W, KWIN), lambda hp: (0, hp, 0, 0)),
                  pl.BlockSpec((2 * GRID_W, KWIN), lambda hp: (0, 0)),
                  pl.BlockSpec((2 * DH, 2 * DH), lambda hp: (0, 0))],
        out_specs=(out_col, out_col, out_col,
                   pl.BlockSpec((1, WIN_H, 2 * GRID_W, KWIN), lambda hp: (hp, 0, 0, 0))),
        scratch_shapes=[pltpu.VMEM((T, 2 * DH), F32), pltpu.VMEM((T, 2 * DH), F32)],
        compiler_params=_params(dimension_semantics=("parallel",)),
    )(qkv, qkv, qkv, datt, bias, jnp.asarray(valid_np), jnp.asarray(hmask_np))


def _conv_taps(up):
    return (_shift_rows(up, 2), _shift_rows(up, 1), up, _shift_rows(up, -1))


def _pair_block_diag(w_pair, dup, same_half):
    return jnp.where(same_half, _dot(w_pair.astype(BF16), dup), 0.0).astype(BF16)


def _gates(u, u16, wa, ba, wi, bi, lam):
    r = _sigmoid(_dot(u16, wa) + ba)
    ig = _sigmoid(_dot(u16, wi) + bi)
    sp = _softplus(-lam)
    log_a = (-LRU_C) * r * sp
    a = jnp.exp(log_a)
    mult = jnp.sqrt(jnp.maximum(-_expm1(2.0 * log_a), 0.0))
    return r, ig, sp, a, mult


def _block_scan(a_ref, b_ref, h_ref, reverse):
    c = a_ref.shape[1]
    nblk = T // 8
    rows = lax.broadcasted_iota(jnp.int32, (8, c), 0)

    def step(i, h_prev):
        blk = (nblk - 1 - i) if reverse else i
        t0 = pl.multiple_of(blk * 8, 8)
        a = a_ref[pl.ds(t0, 8), :]
        b = b_ref[pl.ds(t0, 8), :]
        for s in (1, 2, 4):
            if reverse:
                keep = rows < 8 - s
                a_s = jnp.where(keep, pltpu.roll(a, 8 - s, 0), 1.0)
                b_s = jnp.where(keep, pltpu.roll(b, 8 - s, 0), 0.0)
            else:
                keep = rows >= s
                a_s = jnp.where(keep, pltpu.roll(a, s, 0), 1.0)
                b_s = jnp.where(keep, pltpu.roll(b, s, 0), 0.0)
            b = a * b_s + b
            a = a * a_s
        h = a * h_prev + b
        h_ref[pl.ds(t0, 8), :] = h
        edge = h[0:1] if reverse else h[7:8]
        return jnp.broadcast_to(edge, (8, c))

    lax.fori_loop(0, nblk, step, jnp.zeros((8, c), F32))


def _rec_specs():
    tok = lambda off: pl.BlockSpec((T, CG), lambda g: (0, g + off))
    per_ch = lambda rows: pl.BlockSpec((rows, CG), lambda g: (0, g))
    wspec = pl.BlockSpec((2, 1, CG, REC_BLOCK), lambda g: (0, g, 0, 0))
    const = lambda shape: pl.BlockSpec(shape, lambda g: (0, 0))
    return tok, per_ch, wspec, const


def _rec_fwd(uy, conv_w, conv_b, w_a, b_a, w_i, b_i, lam):
    tok, per_ch, wspec, const = _rec_specs()

    def body(up_ref, yb_ref, cw_ref, cb_ref, wa_ref, ba_ref, wi_ref, bi_ref, lam_ref, dup_ref, half_ref,
             hf_ref, hb_ref, yrec_ref, a_f, bx_f, a_b, bx_b):
        dup = dup_ref[...]
        same_half = half_ref[...] > 0.5
        taps = _conv_taps(up_ref[...])
        u = cb_ref[...]
        for j in range(4):
            u = u + taps[j] * cw_ref[j:j + 1, :]
        u16 = u.astype(BF16)
        for d, (a_s, bx_s) in enumerate(((a_f, bx_f), (a_b, bx_b))):
            wa = _pair_block_diag(wa_ref[d, 0], dup, same_half)
            wi = _pair_block_diag(wi_ref[d, 0], dup, same_half)
            _, ig, _, a, mult = _gates(u, u16, wa, ba_ref[d:d + 1, :], wi, bi_ref[d:d + 1, :],
                                       lam_ref[d:d + 1, :])
            a_s[...] = a
            bx_s[...] = mult * (ig * u)
        _block_scan(a_f, bx_f, hf_ref, False)
        _block_scan(a_b, bx_b, hb_ref, True)
        gelu, _ = _gelu_and_grad(yb_ref[...])
        yrec_ref[...] = ((hf_ref[...] + hb_ref[...]) * gelu).astype(BF16)

    return pl.pallas_call(
        body, name="rec_fwd",
        out_shape=(jax.ShapeDtypeStruct((T, D_REC), F32), jax.ShapeDtypeStruct((T, D_REC), F32),
                   jax.ShapeDtypeStruct((T, D_REC), BF16)),
        grid=(N_CG,),
        in_specs=[tok(0), tok(N_CG), per_ch(4), per_ch(1), wspec, per_ch(2), wspec, per_ch(2), per_ch(2),
                  const((REC_BLOCK, CG)), const((CG, CG))],
        out_specs=(tok(0), tok(0), tok(0)),
        scratch_shapes=[pltpu.VMEM((T, CG), F32)] * 4,
        compiler_params=_params(dimension_semantics=("parallel",)),
    )(uy, uy, conv_w, conv_b, w_a, b_a, w_i, b_i, lam,
      jnp.asarray(_dup_table(), BF16), jnp.asarray(_pair_mask()))


def _rec_bwd(uy, hf, hb, dyrec, conv_w, conv_b, w_a, b_a, w_i, b_i, lam):
    tok, per_ch, wspec, const = _rec_specs()

    def body(up_ref, yb_ref, hf_ref, hb_ref, dy_ref, cw_ref, cb_ref, wa_ref, ba_ref, wi_ref, bi_ref,
             lam_ref, dup_ref, dupt_ref, half_ref,
             dup_out, dyb_ref, dcw_ref, dcb_ref, dwa_ref, dba_ref, dwi_ref, dbi_ref, dlam_ref,
             a_s, dh_s, g_s):
        dup = dup_ref[...]
        dup_t = dupt_ref[...]
        same_half = half_ref[...] > 0.5
        taps = _conv_taps(up_ref[...])
        u = cb_ref[...]
        for j in range(4):
            u = u + taps[j] * cw_ref[j:j + 1, :]
        u16 = u.astype(BF16)
        gelu, dgelu = _gelu_and_grad(yb_ref[...])
        dy = dy_ref[...]
        dyb_ref[...] = (dy * (hf_ref[...] + hb_ref[...]) * dgelu).astype(BF16)
        dh_s[...] = dy * gelu
        du = jnp.zeros((T, CG), F32)
        for d in range(2):
            reverse = d == 1
            wa = _pair_block_diag(wa_ref[d, 0], dup, same_half)
            wi = _pair_block_diag(wi_ref[d, 0], dup, same_half)
            lam_d = lam_ref[d:d + 1, :]
            r, ig, sp, a, mult = _gates(u, u16, wa, ba_ref[d:d + 1, :], wi, bi_ref[d:d + 1, :], lam_d)
            a_s[...] = _shift_rows(a, 1 if reverse else -1)
            _block_scan(a_s, dh_s, g_s, not reverse)
            g = g_s[...]
            h_prev = _shift_rows(hb_ref[...], -1) if reverse else _shift_rows(hf_ref[...], 1)
            da = g * h_prev
            dmult = g * (ig * u)
            dig = g * mult * u
            du = du + g * mult * ig
            dmult_dlog = jnp.where(mult > 0.0, -(a * a) / mult, 0.0)
            dlog_a = da * a + dmult * dmult_dlog
            dr = dlog_a * ((-LRU_C) * sp)
            dsp = jnp.sum(dlog_a * ((-LRU_C) * r), axis=0, keepdims=True)
            dlam_ref[d:d + 1, :] = dsp * (-_sigmoid(-lam_d))
            dga = dr * r * (1.0 - r)
            dgi = dig * ig * (1.0 - ig)
            dga16 = dga.astype(BF16)
            dgi16 = dgi.astype(BF16)
            du = du + _dot_nt(dga16, wa) + _dot_nt(dgi16, wi)
            dwa_ref[d, 0] = _dot_exact(jnp.where(same_half, _dot_tn(u16, dga16), 0.0), dup_t)
            dwi_ref[d, 0] = _dot_exact(jnp.where(same_half, _dot_tn(u16, dgi16), 0.0), dup_t)
            dba_ref[d:d + 1, :] = jnp.sum(dga, axis=0, keepdims=True)
            dbi_ref[d:d + 1, :] = jnp.sum(dgi, axis=0, keepdims=True)
        dcb_ref[...] = jnp.sum(du, axis=0, keepdims=True)
        for j in range(4):
            dcw_ref[j:j + 1, :] = jnp.sum(du * taps[j], axis=0, keepdims=True)
        dup_in = (_shift_rows(du, -2) * cw_ref[0:1, :] + _shift_rows(du, -1) * cw_ref[1:2, :]
                  + du * cw_ref[2:3, :] + _shift_rows(du, 1) * cw_ref[3:4, :])
        dup_out[...] = dup_in.astype(BF16)

    wshape = jax.ShapeDtypeStruct((2, N_CG, CG, REC_BLOCK), F32)
    vec = lambda rows: jax.ShapeDtypeStruct((rows, D_REC), F32)
    dup_np = _dup_table()
    return pl.pallas_call(
        body, name="rec_bwd",
        out_shape=(jax.ShapeDtypeStruct((T, D_REC), BF16), jax.ShapeDtypeStruct((T, D_REC), BF16),
                   vec(4), vec(1), wshape, vec(2), wshape, vec(2), vec(2)),
        grid=(N_CG,),
        in_specs=[tok(0), tok(N_CG), tok(0), tok(0), tok(0),
                  per_ch(4), per_ch(1), wspec, per_ch(2), wspec, per_ch(2), per_ch(2),
                  const((REC_BLOCK, CG)), const((CG, REC_BLOCK)), const((CG, CG))],
        out_specs=(tok(0), tok(0), per_ch(4), per_ch(1), wspec, per_ch(2), wspec, per_ch(2), per_ch(2)),
        scratch_shapes=[pltpu.VMEM((T, CG), F32)] * 3,
        compiler_params=_params(dimension_semantics=("parallel",)),
    )(uy, uy, hf, hb, dyrec, conv_w, conv_b, w_a, b_a, w_i, b_i, lam,
      jnp.asarray(dup_np, BF16), jnp.asarray(dup_np.T.copy()), jnp.asarray(_pair_mask()))


TM_MIX = 256


def _mix_specs():
    tok = lambda width, blk=0: pl.BlockSpec((TM_MIX, width), lambda i: (i, blk))
    full = lambda shape: pl.BlockSpec(shape, lambda i: (0, 0))
    return tok, full


def _mix_fwd(x, att, yrec, gg, w_att_o_t, w_rec_o, w_out):
    tok, full = _mix_specs()

    def body(x_ref, att_ref, yr_ref, ga_ref, gr_ref, wao_ref, wro_ref, wo_ref, x1_ref, mixed_ref):
        y_att = _dot_nt(att_ref[...], wao_ref[...])
        y_rec = _dot(yr_ref[...], wro_ref[...])
        mixed = (_sigmoid(ga_ref[...]) * y_att + _sigmoid(gr_ref[...]) * y_rec).astype(BF16)
        mixed_ref[...] = mixed
        x1_ref[...] = x_ref[...] + _dot(mixed, wo_ref[...])

    return pl.pallas_call(
        body, name="mix_fwd",
        out_shape=(jax.ShapeDtypeStruct((T, D), F32), jax.ShapeDtypeStruct((T, D), BF16)),
        grid=(T // TM_MIX,),
        in_specs=[tok(D), tok(D_ATT), tok(D_REC), tok(D, 0), tok(D, 1),
                  full((D, D_ATT)), full((D_REC, D)), full((D, D))],
        out_specs=(tok(D), tok(D)),
        compiler_params=_params(dimension_semantics=("parallel",)),
    )(x, att, yrec, gg, gg, w_att_o_t, w_rec_o, w_out)


def _mix_bwd(dx1, att, yrec, gg, w_att_o_t, w_rec_o, w_out):
    tok, full = _mix_specs()

    def body(dx_ref, att_ref, yr_ref, ga_ref, gr_ref, wao_ref, wro_ref, wo_ref,
             dga_ref, dgr_ref, dya_ref, dyr_ref, datt_ref, dyrp_ref):
        dmixed = _dot_nt(dx_ref[...].astype(BF16), wo_ref[...])
        y_att = _dot_nt(att_ref[...], wao_ref[...])
        y_rec = _dot(yr_ref[...], wro_ref[...])
        sa = _sigmoid(ga_ref[...])
        sr = _sigmoid(gr_ref[...])
        dga_ref[...] = (dmixed * y_att * sa * (1.0 - sa)).astype(BF16)
        dgr_ref[...] = (dmixed * y_rec * sr * (1.0 - sr)).astype(BF16)
        dya = (dmixed * sa).astype(BF16)
        dyr = (dmixed * sr).astype(BF16)
        dya_ref[...] = dya
        dyr_ref[...] = dyr
        datt_ref[...] = _dot(dya, wao_ref[...]).astype(BF16)
        dyrp_ref[...] = _dot_nt(dyr, wro_ref[...])

    return pl.pallas_call(
        body, name="mix_bwd",
        out_shape=(jax.ShapeDtypeStruct((T, D), BF16), jax.ShapeDtypeStruct((T, D), BF16),
                   jax.ShapeDtypeStruct((T, D), BF16), jax.ShapeDtypeStruct((T, D), BF16),
                   jax.ShapeDtypeStruct((T, D_ATT), BF16), jax.ShapeDtypeStruct((T, D_REC), F32)),
        grid=(T // TM_MIX,),
        in_specs=[tok(D), tok(D_ATT), tok(D_REC), tok(D, 0), tok(D, 1),
                  full((D, D_ATT)), full((D_REC, D)), full((D, D))],
        out_specs=(tok(D), tok(D), tok(D), tok(D), tok(D_ATT), tok(D_REC)),
        compiler_params=_params(dimension_semantics=("parallel",)),
    )(dx1, att, yrec, gg, gg, w_att_o_t, w_rec_o, w_out)


TM_FFN = 256
FF_CHUNK = 1024


def _ffn_loss(x1, target, g2, gf, w_ff1_t, w_ff2):
    n_chunks = D_FF // FF_CHUNK

    def body(x1_ref, tg_ref, g2_ref, gf_ref, w1_hbm, w2_hbm,
             loss_ref, dx1_ref, h2_ref, act_ref, dpre_ref, dx2_ref, dg2_ref, dgf_ref,
             w1, w2, relu_s):
        i = pl.program_id(0)

        @pl.when(i == 0)
        def _():
            pltpu.sync_copy(w1_hbm, w1)
            pltpu.sync_copy(w2_hbm, w2)
            loss_ref[...] = jnp.zeros_like(loss_ref)
            dg2_ref[...] = jnp.zeros_like(dg2_ref)
            dgf_ref[...] = jnp.zeros_like(dgf_ref)

        x1v = x1_ref[...]
        r2 = lax.rsqrt(jnp.mean(x1v * x1v, axis=-1, keepdims=True) + EPS)
        xh2 = x1v * r2
        h2 = (xh2 * g2_ref[...]).astype(BF16)
        h2_ref[...] = h2
        x2 = x1v
        for c in range(n_chunks):
            ff = slice(c * FF_CHUNK, (c + 1) * FF_CHUNK)
            rl = jnp.maximum(_dot_nt(h2, w1[ff, :]), 0.0)
            relu_s[:, ff] = rl
            act = (rl * rl).astype(BF16)
            act_ref[:, ff] = act
            x2 = x2 + _dot(act, w2[ff, :])
        r3 = lax.rsqrt(jnp.mean(x2 * x2, axis=-1, keepdims=True) + EPS)
        xh3 = x2 * r3
        err = xh3 * gf_ref[...] - tg_ref[...]
        loss_ref[...] += 0.5 * jnp.sum(jnp.mean(err * err, axis=-1, keepdims=True))
        dy = err * (1.0 / D)
        dgf_ref[...] += jnp.sum(dy * xh3, axis=0, keepdims=True)
        dx2 = _rms_bwd(dy, xh3, r3, gf_ref[...])
        dx2_16 = dx2.astype(BF16)
        dx2_ref[...] = dx2_16
        dh2 = jnp.zeros((TM_FFN, D), F32)
        for c in range(n_chunks):
            ff = slice(c * FF_CHUNK, (c + 1) * FF_CHUNK)
            dpre = (_dot_nt(dx2_16, w2[ff, :]) * (2.0 * relu_s[:, ff])).astype(BF16)
            dpre_ref[:, ff] = dpre
            dh2 = dh2 + _dot(dpre, w1[ff, :])
        dg2_ref[...] += jnp.sum(dh2 * xh2, axis=0, keepdims=True)
        dx1_ref[...] = dx2 + _rms_bwd(dh2, xh2, r2, g2_ref[...])

    tok = lambda width: pl.BlockSpec((TM_FFN, width), lambda i: (i, 0))
    vec = pl.BlockSpec((1, D), lambda i: (0, 0))
    hbm = pl.BlockSpec(memory_space=pl.ANY)
    return pl.pallas_call(
        body, name="ffn_loss",
        out_shape=(jax.ShapeDtypeStruct((8, 128), F32), jax.ShapeDtypeStruct((T, D), F32),
                   jax.ShapeDtypeStruct((T, D), BF16), jax.ShapeDtypeStruct((T, D_FF), BF16),
                   jax.ShapeDtypeStruct((T, D_FF), BF16), jax.ShapeDtypeStruct((T, D), BF16),
                   jax.ShapeDtypeStruct((1, D), F32), jax.ShapeDtypeStruct((1, D), F32)),
        grid=(T // TM_FFN,),
        in_specs=[tok(D), tok(D), vec, vec, hbm, hbm],
        out_specs=(pl.BlockSpec((8, 128), lambda i: (0, 0)), tok(D), tok(D), tok(D_FF), tok(D_FF), tok(D),
                   vec, vec),
        scratch_shapes=[pltpu.VMEM((D_FF, D), BF16), pltpu.VMEM((D_FF, D), BF16),
                        pltpu.VMEM((TM_FFN, D_FF), F32)],
        compiler_params=_params(dimension_semantics=("arbitrary",)),
    )(x1, target, g2, gf, w_ff1_t, w_ff2)


def _local_step(x, target, p):
    bias = _rpb_expand(p["rpb"])
    pairs = lambda w: w.reshape(2, N_CG, CG, REC_BLOCK)
    w_a, w_i = pairs(p["w_rg_a"]), pairs(p["w_rg_i"])
    rec_params = (p["conv_w"], p["conv_b"], w_a, p["b_rg_a"], w_i, p["b_rg_i"], p["lru_lambda"])

    qkv, uy, gg, h = _in_proj(x, p["ln1_g"], p["w_in_t"], p["b_in"])
    att = _att_fwd(qkv, bias)
    hf, hb, yrec = _rec_fwd(uy, *rec_params)
    x1, mixed = _mix_fwd(x, att, yrec, gg, p["w_att_o_t"], p["w_rec_o"], p["w_out"])
    loss8, dx1, h2, act, dpre, dx2, g_ln2, g_lnf = _ffn_loss(
        x1, target, p["ln2_g"], p["lnf_g"], p["w_ff1_t"], p["w_ff2"])

    dga, dgr, dya, dyr, datt, dyrp = _mix_bwd(dx1, att, yrec, gg, p["w_att_o_t"], p["w_rec_o"], p["w_out"])
    dup, dyb, g_cw, g_cb, g_wa, g_ba, g_wi, g_bi, g_lam = _rec_bwd(uy, hf, hb, dyrp, *rec_params)
    dq, dk, dv, gbias = _att_bwd(qkv, bias, datt)
    dz = (dq, dk, dv, dup, dyb, dga, dgr)
    grad_x, g_ln1 = _dh_norm1_bwd(dz, p["w_in_t"], x, p["ln1_g"], dx1)
    g_w_in_t, g_b_in = _grad_w_in(dz, h)

    blocks = lambda g: g.reshape(2, N_REC_BLOCKS, REC_BLOCK, REC_BLOCK)
    grads = {
        "ln1_g": g_ln1, "w_in_t": g_w_in_t, "b_in": g_b_in, "rpb": _rpb_reduce(gbias),
        "w_att_o_t": _matmul(dya, att, "tn", BF16, "g_w_att_o"),
        "conv_w": g_cw, "conv_b": g_cb, "w_rg_a": blocks(g_wa), "b_rg_a": g_ba,
        "w_rg_i": blocks(g_wi), "b_rg_i": g_bi, "lru_lambda": g_lam,
        "w_rec_o": _matmul(yrec, dyr, "tn", BF16, "g_w_rec_o"),
        "w_out": _matmul(mixed, dx1, "tn", BF16, "g_w_out"),
        "ln2_g": g_ln2,
        "w_ff1_t": _matmul(dpre, h2, "tn", BF16, "g_w_ff1"),
        "w_ff2": _matmul(act, dx2, "tn", BF16, "g_w_ff2"),
        "lnf_g": g_lnf,
    }
    return loss8[0:1, 0:1], grad_x, grads


MESH_ID = pl.DeviceIdType.MESH
ANY = pl.BlockSpec(memory_space=pl.ANY)

SECTIONS = (("w_in_t", 704, D), ("w_rec_o", 128, D), ("w_out", 128, D), ("w_ff1_t", 512, D),
            ("w_ff2", 512, D), ("chan", 16, D), ("w_att_o_t", 128, D_ATT))
N_SEC = len(SECTIONS)
MAIN_ROWS = sum(r for _, r, c in SECTIONS if c == D)
ROW_TILE = 400
N_CHAN_ROWS = 10
CHAN = (("conv_w", 4), ("b_rg_a", 2), ("b_rg_i", 2), ("lru_lambda", 2))


def _position():
    return lax.axis_index("x"), lax.axis_index("y"), lax.axis_index("c")


def _other_chips(x, y):
    return [(1 - x, y), (x, 1 - y), (1 - x, 1 - y)]


def _block_of(ref, dev, rows):
    return ref.at[pl.ds(pl.multiple_of(dev * rows, 16), rows)]


def _all_gather(shards, name):
    ns = len(shards)

    def body(*refs):
        x_refs, out_refs = refs[:ns], refs[ns:2 * ns]
        send_sems, recv_sems, local_sems = refs[2 * ns:]
        x, y, c = _position()
        me, sibling = (x, y, c), (x, y, 1 - c)
        chips = _other_chips(x, y)

        def rows(s, px, py, pc):
            return _block_of(out_refs[s], 4 * px + 2 * py + pc, shards[s].shape[0])

        def copy(k, s, block, to, from_shard=False):
            return pltpu.make_async_remote_copy(
                src_ref=x_refs[s] if from_shard else rows(s, *block), dst_ref=rows(s, *block),
                send_sem=send_sems.at[k * ns + s], recv_sem=recv_sems.at[k * ns + s],
                device_id=to, device_id_type=MESH_ID)

        sections = range(ns)
        mine = [pltpu.make_async_copy(x_refs[s], rows(s, *me), local_sems.at[s]) for s in sections]
        first = [copy(0, s, me, sibling, True) for s in sections]
        first += [copy(1 + j, s, me, (*chip, c), True) for j, chip in enumerate(chips) for s in sections]
        for cp in mine + first:
            cp.start()
        passed = []
        for j, chip in enumerate(chips):
            for s in sections:
                copy(1 + j, s, (*chip, c), me).wait_recv()
                passed.append(copy(4 + j, s, (*chip, c), sibling))
                passed[-1].start()
        for s in sections:
            copy(0, s, sibling, me).wait_recv()
        for j, chip in enumerate(chips):
            for s in sections:
                copy(4 + j, s, (*chip, 1 - c), me).wait_recv()
        for cp in first + passed:
            cp.wait_send()
        for cp in mine:
            cp.wait()

    return pl.pallas_call(
        body, name=name,
        out_shape=tuple(jax.ShapeDtypeStruct((N_DEV * s.shape[0], s.shape[1]), s.dtype) for s in shards),
        in_specs=[ANY] * ns, out_specs=(ANY,) * ns,
        scratch_shapes=[pltpu.SemaphoreType.DMA((7 * ns,)), pltpu.SemaphoreType.DMA((7 * ns,)),
                        pltpu.SemaphoreType.DMA((ns,))],
    )(*shards)


def _landing(slots):
    return (jax.ShapeDtypeStruct((slots, MAIN_ROWS, D), BF16),
            jax.ShapeDtypeStruct((slots, SECTIONS[-1][1], D_ATT), BF16))


def _section_dst(main_ref, att_ref, slot, s, row0):
    _, rows, cols = SECTIONS[s]
    if cols == D:
        return main_ref.at[slot, pl.ds(row0, rows)]
    return att_ref.at[slot]


def _pair_exchange(grads):
    def body(*refs):
        g_refs = refs[:N_SEC]
        own_main, own_att, recv_main, recv_att, send_sems, recv_sems, local_sems = refs[N_SEC:]
        x, y, c = _position()
        remote, local = [], []
        for k in range(N_CHIPS):
            row0 = 0
            for s, (_, rows, cols) in enumerate(SECTIONS):
                n = k * N_SEC + s
                local.append(pltpu.make_async_copy(
                    _block_of(g_refs[s], 2 * k + c, rows),
                    _section_dst(own_main, own_att, k, s, row0), local_sems.at[n]))
                remote.append(pltpu.make_async_remote_copy(
                    src_ref=_block_of(g_refs[s], 2 * k + 1 - c, rows),
                    dst_ref=_section_dst(recv_main, recv_att, k, s, row0),
                    send_sem=send_sems.at[n], recv_sem=recv_sems.at[n],
                    device_id=(x, y, 1 - c), device_id_type=MESH_ID))
                row0 += rows if cols == D else 0
        for cp in remote + local:
            cp.start()
        for cp in remote:
            cp.wait_recv()
        for cp in remote:
            cp.wait_send()
        for cp in local:
            cp.wait()

    n = N_CHIPS * N_SEC
    return pl.pallas_call(
        body, name="grad_pair_exchange",
        out_shape=_landing(N_CHIPS) + _landing(N_CHIPS),
        in_specs=[ANY] * N_SEC, out_specs=(ANY,) * 4,
        scratch_shapes=[pltpu.SemaphoreType.DMA((n,)), pltpu.SemaphoreType.DMA((n,)),
                        pltpu.SemaphoreType.DMA((n,))],
    )(*grads)


def _row_tile(rows):
    return ROW_TILE if rows % ROW_TILE == 0 else rows


def _pair_add(own, got, name):
    slots, r, c_dim = own.shape
    tr = _row_tile(r)

    def body(a_ref, b_ref, p_ref):
        p_ref[...] = (a_ref[...].astype(F32) + b_ref[...].astype(F32)).astype(BF16)

    spec = pl.BlockSpec((1, tr, c_dim), lambda k, i: (k, i, 0))
    return pl.pallas_call(
        body, name=name,
        out_shape=jax.ShapeDtypeStruct(own.shape, BF16),
        grid=(slots, r // tr),
        in_specs=[spec, spec], out_specs=spec,
        compiler_params=_params(dimension_semantics=("parallel", "parallel")),
    )(own, got)


def _chip_exchange(p_main, p_att):
    def body(pm_ref, pa_ref, bm_ref, ba_ref, send_sems, recv_sems):
        x, y, c = _position()
        copies = []
        for j, (cx, cy) in enumerate(_other_chips(x, y)):
            for t, (src, dst) in enumerate(((pm_ref, bm_ref), (pa_ref, ba_ref))):
                copies.append(pltpu.make_async_remote_copy(
                    src_ref=src.at[2 * cx + cy], dst_ref=dst.at[j],
                    send_sem=send_sems.at[2 * j + t], recv_sem=recv_sems.at[2 * j + t],
                    device_id=(cx, cy, c), device_id_type=MESH_ID))
        for cp in copies:
            cp.start()
        for cp in copies:
            cp.wait_recv()
        for cp in copies:
            cp.wait_send()

    return pl.pallas_call(
        body, name="grad_chip_exchange",
        out_shape=_landing(3),
        in_specs=[ANY, ANY], out_specs=(ANY, ANY),
        scratch_shapes=[pltpu.SemaphoreType.DMA((6,)), pltpu.SemaphoreType.DMA((6,))],
    )(p_main, p_att)


def _grad_finish(p, b, chip, name):
    _, r, c_dim = p.shape
    tr = _row_tile(r)

    def body(chip_ref, p_ref, b_ref, g_ref):
        g = p_ref[0].astype(F32)
        for j in range(3):
            g = g + b_ref[j].astype(F32)
        g_ref[...] = g

    return pl.pallas_call(
        body, name=name,
        out_shape=jax.ShapeDtypeStruct((r, c_dim), F32),
        grid_spec=pltpu.PrefetchScalarGridSpec(
            num_scalar_prefetch=1, grid=(r // tr,),
            in_specs=[pl.BlockSpec((1, tr, c_dim), lambda i, s: (s[0], i, 0)),
                      pl.BlockSpec((3, tr, c_dim), lambda i, s: (0, i, 0))],
            out_specs=pl.BlockSpec((tr, c_dim), lambda i, s: (i, 0))),
        compiler_params=_params(dimension_semantics=("parallel",)),
    )(chip, p, b)


def _sum_devices(parts, rows):
    tr = rows // 2

    def body(*refs):
        s = refs[0][...]
        for d in range(1, N_DEV):
            s = s + refs[d][...]
        refs[N_DEV][...] = s

    return pl.pallas_call(
        body, name="small_grad_sum",
        out_shape=jax.ShapeDtypeStruct((rows, LANES), F32),
        grid=(2,),
        in_specs=[pl.BlockSpec((tr, LANES), lambda i, d=d: (2 * d + i, 0)) for d in range(N_DEV)],
        out_specs=pl.BlockSpec((tr, LANES), lambda i: (i, 0)),
        compiler_params=_params(dimension_semantics=("parallel",)),
    )(*([parts] * N_DEV))


def _adamw(w, g, m, v, name):
    rows, cols = w.shape
    tr = rows
    while tr * cols * 4 > (1 << 20) and tr % 16 == 0:
        tr //= 2
    c1 = 1.0 / (1.0 - ADAM_B1 ** ADAM_STEP)
    c2 = 1.0 / (1.0 - ADAM_B2 ** ADAM_STEP)

    def body(w_ref, g_ref, m_ref, v_ref, d_ref, nm_ref, nv_ref):
        gv = g_ref[...]
        nm = ADAM_B1 * m_ref[...] + (1.0 - ADAM_B1) * gv
        nv = ADAM_B2 * v_ref[...] + (1.0 - ADAM_B2) * (gv * gv)
        nm_ref[...] = nm
        nv_ref[...] = nv
        d_ref[...] = (-ADAM_LR) * ((nm * c1) / (jnp.sqrt(nv * c2) + ADAM_EPS) + ADAM_WD * w_ref[...])

    spec = pl.BlockSpec((tr, cols), lambda i: (i, 0))
    shape = jax.ShapeDtypeStruct((rows, cols), F32)
    return pl.pallas_call(
        body, name=name,
        out_shape=(shape, shape, shape),
        grid=(rows // tr,),
        in_specs=[spec] * 4, out_specs=(spec,) * 3,
        compiler_params=_params(dimension_semantics=("parallel",)),
    )(w, g, m, v)


NAMES = ("ln1_g", "w_in", "b_in", "rpb", "w_att_o", "conv_w", "conv_b", "w_rg_a", "b_rg_a", "w_rg_i",
         "b_rg_i", "lru_lambda", "w_rec_o", "w_out", "ln2_g", "w_ff1", "w_ff2", "lnf_g")
TRANSPOSED = {"w_in": "w_in_t", "w_att_o": "w_att_o_t", "w_ff1": "w_ff1_t"}
ROW_SHARDED = ("w_rec_o", "w_out", "w_ff2")
REPLICATED = (("ln1_g", (1, D)), ("b_in", (1, D_IN)), ("rpb", (N_HEADS * N_RPB_R, N_RPB_C)),
              ("conv_b", (1, D_REC)), ("w_rg_a", (2 * N_REC_BLOCKS * REC_BLOCK, REC_BLOCK)),
              ("w_rg_i", (2 * N_REC_BLOCKS * REC_BLOCK, REC_BLOCK)), ("ln2_g", (1, D)), ("lnf_g", (1, D)))
SMALL_ROWS = 2160


def _chan_bits(vectors):
    chan = jnp.concatenate(vectors, axis=0)
    bits = lax.bitcast_convert_type(chan, BF16).reshape(-1)
    return jnp.pad(bits, (0, 16 * D - bits.shape[0])).reshape(16, D)


def _chan_from_bits(gathered):
    bits = gathered.reshape(N_DEV, 16 * D)[:, :2 * N_CHAN_ROWS * LANES]
    chan = lax.bitcast_convert_type(bits.reshape(N_DEV, N_CHAN_ROWS, LANES, 2), F32)
    return chan.transpose(1, 0, 2).reshape(N_CHAN_ROWS, D)


def kernel(x, ln1_g, w_in, b_in, rpb, w_att_o, conv_w, conv_b, w_rg_a, b_rg_a, w_rg_i, b_rg_i, lru_lambda, w_rec_o, w_out, ln2_g, w_ff1, w_ff2, lnf_g, loss_target, m_ln1_g, m_w_in, m_b_in, m_rpb, m_w_att_o, m_conv_w, m_conv_b, m_w_rg_a, m_b_rg_a, m_w_rg_i, m_b_rg_i, m_lru_lambda, m_w_rec_o, m_w_out, m_ln2_g, m_w_ff1, m_w_ff2, m_lnf_g, v_ln1_g, v_w_in, v_b_in, v_rpb, v_w_att_o, v_conv_w, v_conv_b, v_w_rg_a, v_b_rg_a, v_w_rg_i, v_b_rg_i, v_lru_lambda, v_w_rec_o, v_w_out, v_ln2_g, v_w_ff1, v_w_ff2, v_lnf_g):
    w = dict(zip(NAMES, (ln1_g, w_in, b_in, rpb, w_att_o, conv_w, conv_b, w_rg_a, b_rg_a, w_rg_i,
                         b_rg_i, lru_lambda, w_rec_o, w_out, ln2_g, w_ff1, w_ff2, lnf_g)))
    m = dict(zip(NAMES, (m_ln1_g, m_w_in, m_b_in, m_rpb, m_w_att_o, m_conv_w, m_conv_b, m_w_rg_a,
                         m_b_rg_a, m_w_rg_i, m_b_rg_i, m_lru_lambda, m_w_rec_o, m_w_out, m_ln2_g,
                         m_w_ff1, m_w_ff2, m_lnf_g)))
    v = dict(zip(NAMES, (v_ln1_g, v_w_in, v_b_in, v_rpb, v_w_att_o, v_conv_w, v_conv_b, v_w_rg_a,
                         v_b_rg_a, v_w_rg_i, v_b_rg_i, v_lru_lambda, v_w_rec_o, v_w_out, v_ln2_g,
                         v_w_ff1, v_w_ff2, v_lnf_g)))
    xi, yi, _ = _position()

    shard = {t: w[n][0].T.astype(BF16) for n, t in TRANSPOSED.items()}
    shard.update({n: w[n][0].astype(BF16) for n in ROW_SHARDED})
    shard["chan"] = _chan_bits([w[n][0] for n, _ in CHAN])
    gathered = dict(zip((n for n, _, _ in SECTIONS),
                        _all_gather([shard[n] for n, _, _ in SECTIONS], "weight_all_gather")))
    chan = _chan_from_bits(gathered.pop("chan"))
    p = dict(gathered)
    r0 = 0
    for n, rows in CHAN:
        p[n] = chan[r0:r0 + rows]
        r0 += rows
    p.update(ln1_g=w["ln1_g"], b_in=w["b_in"], rpb=w["rpb"][0], conv_b=w["conv_b"],
             w_rg_a=w["w_rg_a"][0], w_rg_i=w["w_rg_i"][0], ln2_g=w["ln2_g"],
             lnf_g=w["lnf_g"].reshape(1, D))

    loss_part, grad_x, grads = _local_step(x[0], loss_target[0], p)

    chan_g = jnp.concatenate([grads[n] for n, _ in CHAN], axis=0)
    chan_g = chan_g.reshape(N_CHAN_ROWS, N_DEV, LANES).transpose(1, 0, 2).astype(BF16)
    chan_g = jnp.pad(chan_g.reshape(N_DEV, -1), ((0, 0), (0, 16 * D - N_CHAN_ROWS * LANES)))
    grads["chan"] = chan_g.reshape(N_DEV * 16, D)
    own_main, own_att, got_main, got_att = _pair_exchange([grads[n] for n, _, _ in SECTIONS])
    part_main = _pair_add(own_main, got_main, "grad_pair_add")
    part_att = _pair_add(own_att, got_att, "grad_pair_add_att")
    far_main, far_att = _chip_exchange(part_main, part_att)
    chip = jnp.reshape(2 * xi + yi, (1,)).astype(jnp.int32)
    g_main = _grad_finish(part_main, far_main, chip, "grad_finish")
    g_att = _grad_finish(part_att, far_att, chip, "grad_finish_att")

    flat = jnp.concatenate([grads[n].reshape(-1) for n, _ in REPLICATED] + [loss_part.reshape(-1)])
    n_small = flat.shape[0]
    flat = jnp.pad(flat, (0, SMALL_ROWS * LANES - n_small)).reshape(SMALL_ROWS, LANES)
    (small_parts,) = _all_gather([flat], "small_grad_all_gather")
    small = _sum_devices(small_parts, SMALL_ROWS).reshape(-1)
    loss = small[n_small - 1]

    g, delta, new_m, new_v = {}, {}, {}, {}

    def update(n, g2, shape2):
        d2, m2, v2 = _adamw(w[n].reshape(shape2), g2, m[n].reshape(shape2), v[n].reshape(shape2),
                            "adamw_" + n)
        g[n], delta[n], new_m[n], new_v[n] = (a.reshape(w[n].shape) for a in (g2, d2, m2, v2))

    r0 = 0
    for n, rows, cols in SECTIONS[:-1]:
        blk = g_main[r0:r0 + rows]
        r0 += rows
        if n == "chan":
            chan_back = blk.reshape(-1)[:N_CHAN_ROWS * LANES].reshape(N_CHAN_ROWS, LANES)
        elif n in ROW_SHARDED:
            update(n, blk, blk.shape)
        else:
            name = [k for k, t in TRANSPOSED.items() if t == n][0]
            update(name, blk.T, (cols, rows))
    update("w_att_o", g_att.T, (D_ATT, SECTIONS[-1][1]))
    r0 = 0
    for n, rows in CHAN:
        update(n, chan_back[r0:r0 + rows], (rows, LANES))
        r0 += rows
    o = 0
    for n, shape2 in REPLICATED:
        size = shape2[0] * shape2[1]
        update(n, small[o:o + size].reshape(shape2), shape2)
        o += size

    return (loss, grad_x[None], *[g[n] for n in NAMES], *[delta[n] for n in NAMES],
            *[new_m[n] for n in NAMES], *[new_v[n] for n in NAMES])
```

```python
import math

import numpy as np
import jax
import jax.numpy as jnp
from jax import lax
from jax.experimental import pallas as pl
from jax.experimental.pallas import tpu as pltpu

F32 = jnp.float32
BF16 = jnp.bfloat16

T = 2048
D = 1024
D_ATT = 512
D_REC = 1024
D_FF = 4096
D_IN = 5632
N_HEADS = 8
DH = 64
GRID_W = 64
ROWS = T // GRID_W
WIN_H = 8
WIN_W = 16
KWIN = WIN_H * GRID_W
N_RPB_R = 2 * WIN_H - 1
N_RPB_C = 2 * WIN_W - 1
N_REC_BLOCKS = 16
REC_BLOCK = 64
CG = 128
N_CG = D_REC // CG
LRU_C = 8.0
EPS = 1e-6
N_DEV = 8
N_CHIPS = 4
LANES = 128

ADAM_LR = 0.001
ADAM_B1 = 0.9
ADAM_B2 = 0.999
ADAM_EPS = 1e-08
ADAM_WD = 0.01
ADAM_STEP = 10

MESH_AXES = ("x", "y", "c")
VMEM_LIMIT = 56 * 1024 * 1024

TILE = 512
DZ_SEGMENTS = ((0, 1), (1, 1), (2, 1), (3, 2), (5, 2), (7, 2), (9, 2))
N_DZ_TILES = D_IN // TILE


def _params(**kw):
    return pltpu.CompilerParams(vmem_limit_bytes=VMEM_LIMIT, **kw)


def _att_tables():
    rq = np.arange(2 * GRID_W) % GRID_W
    kc = np.arange(KWIN) % GRID_W
    win_start = np.clip(rq - WIN_W // 2, 0, GRID_W - WIN_W)
    valid = (kc[None, :] >= win_start[:, None]) & (kc[None, :] < win_start[:, None] + WIN_W)
    return valid.astype(np.float32), _pair_mask()


def _pair_mask():
    half = np.arange(2 * DH) // DH
    return (half[:, None] == half[None, :]).astype(np.float32)


def _dup_table():
    return np.concatenate([np.eye(REC_BLOCK, dtype=np.float32)] * 2, axis=1)


def _toeplitz_table():
    q = np.arange(GRID_W)[:, None]
    kc = np.arange(GRID_W)[None, :]
    dc = (kc - q + WIN_W - 1).reshape(-1)
    e = np.zeros((128, GRID_W * GRID_W), np.float32)
    ok = (dc >= 0) & (dc < N_RPB_C)
    e[dc[ok], np.arange(GRID_W * GRID_W)[ok]] = 1.0
    return e


def _row_shift_table():
    s = np.zeros((16, WIN_H * WIN_H), np.float32)
    for oi in range(WIN_H):
        for i in range(WIN_H):
            s[i - oi + WIN_H - 1, oi * WIN_H + i] = 1.0
    return s


def _sigmoid(x):
    return 1.0 / (1.0 + jnp.exp(-x))


def _softplus(x):
    return jnp.maximum(x, 0.0) + jnp.log(1.0 + jnp.exp(-jnp.abs(x)))


def _expm1(x):
    series = x * (1.0 + x * (0.5 + x * (1.0 / 6.0 + x * (1.0 / 24.0))))
    return jnp.where(jnp.abs(x) < 0.02, series, jnp.exp(x) - 1.0)


_GELU_C = math.sqrt(2.0 / math.pi)


def _gelu_and_grad(x):
    x2 = x * x
    inner = _GELU_C * (x + 0.044715 * x * x2)
    t = jnp.tanh(inner)
    g = 0.5 * x * (1.0 + t)
    dg = 0.5 * (1.0 + t) + 0.5 * x * (1.0 - t * t) * _GELU_C * (1.0 + 3.0 * 0.044715 * x2)
    return g, dg


def _dot(a, b):
    return jnp.dot(a, b, preferred_element_type=F32)


def _dot_nt(a, b):
    return lax.dot_general(a, b, (((1,), (1,)), ((), ())), preferred_element_type=F32)


def _dot_tn(a, b):
    return lax.dot_general(a, b, (((0,), (0,)), ((), ())), preferred_element_type=F32)


def _dot_exact(a, b):
    return jnp.dot(a, b, precision=lax.Precision.HIGHEST, preferred_element_type=F32)


def _shift_rows(x, s):
    n = x.shape[0]
    rows = lax.broadcasted_iota(jnp.int32, x.shape, 0)
    y = pltpu.roll(x, s % n, 0)
    if s > 0:
        return jnp.where(rows >= s, y, 0.0)
    return jnp.where(rows < n + s, y, 0.0)


def _rms_bwd(dh, xh, r, g):
    dxh = dh * g
    return r * (dxh - xh * jnp.mean(dxh * xh, axis=-1, keepdims=True))


def _matmul(a, b, mode, out_dtype, name, tm=512, tn=1024, tk=2048):
    if mode == "nn":
        (m, k), (k2, n) = a.shape, b.shape
    elif mode == "nt":
        (m, k), (n, k2) = a.shape, b.shape
    else:
        (k, m), (k2, n) = a.shape, b.shape
    assert k == k2
    tm, tn, tk = min(tm, m), min(tn, n), min(tk, k)
    assert m % tm == 0 and n % tn == 0 and k % tk == 0
    nk = k // tk
    dot = {"nn": _dot, "nt": _dot_nt, "tn": _dot_tn}[mode]

    def body(a_ref, b_ref, o_ref, acc):
        kk = pl.program_id(2)
        part = dot(a_ref[...].astype(BF16), b_ref[...].astype(BF16))
        if nk == 1:
            o_ref[...] = part.astype(out_dtype)
            return

        @pl.when(kk == 0)
        def _():
            acc[...] = part

        @pl.when(kk > 0)
        def _():
            acc[...] += part

        @pl.when(kk == nk - 1)
        def _():
            o_ref[...] = acc[...].astype(out_dtype)

    if mode == "tn":
        a_spec = pl.BlockSpec((tk, tm), lambda i, j, kk: (kk, i))
    else:
        a_spec = pl.BlockSpec((tm, tk), lambda i, j, kk: (i, kk))
    if mode == "nt":
        b_spec = pl.BlockSpec((tn, tk), lambda i, j, kk: (j, kk))
    else:
        b_spec = pl.BlockSpec((tk, tn), lambda i, j, kk: (kk, j))
    return pl.pallas_call(
        body, name=name,
        out_shape=jax.ShapeDtypeStruct((m, n), out_dtype),
        grid=(m // tm, n // tn, nk),
        in_specs=[a_spec, b_spec],
        out_specs=pl.BlockSpec((tm, tn), lambda i, j, kk: (i, j)),
        scratch_shapes=[pltpu.VMEM((tm, tn) if nk > 1 else (8, LANES), F32)],
        compiler_params=_params(dimension_semantics=("parallel", "parallel", "arbitrary")),
    )(a, b)


def _in_proj(x, g1, w_in_t, b_in):
    tm = 512

    def body(x_ref, g_ref, w_ref, b_ref, qkv_ref, uy_ref, gg_ref, h_ref, h_scr):
        j = pl.program_id(1)

        @pl.when(j == 0)
        def _():
            xv = x_ref[...]
            r = lax.rsqrt(jnp.mean(xv * xv, axis=-1, keepdims=True) + EPS)
            h = ((xv * r) * g_ref[...]).astype(BF16)
            h_scr[...] = h
            h_ref[...] = h

        z = _dot_nt(h_scr[...], w_ref[...]) + b_ref[...]

        @pl.when(j < 3)
        def _():
            qkv_ref[...] = z.astype(BF16)

        @pl.when((j >= 3) & (j < 7))
        def _():
            uy_ref[...] = z

        @pl.when(j >= 7)
        def _():
            gg_ref[...] = z

    return pl.pallas_call(
        body, name="in_proj",
        out_shape=(jax.ShapeDtypeStruct((T, 3 * D_ATT), BF16),
                   jax.ShapeDtypeStruct((T, 2 * D_REC), F32),
                   jax.ShapeDtypeStruct((T, 2 * D), F32),
                   jax.ShapeDtypeStruct((T, D), BF16)),
        grid=(T // tm, N_DZ_TILES),
        in_specs=[pl.BlockSpec((tm, D), lambda i, j: (i, 0)),
                  pl.BlockSpec((1, D), lambda i, j: (0, 0)),
                  pl.BlockSpec((TILE, D), lambda i, j: (j, 0)),
                  pl.BlockSpec((1, TILE), lambda i, j: (0, j))],
        out_specs=(pl.BlockSpec((tm, TILE), lambda i, j: (i, jnp.minimum(j, 2))),
                   pl.BlockSpec((tm, TILE), lambda i, j: (i, jnp.clip(j - 3, 0, 3))),
                   pl.BlockSpec((tm, TILE), lambda i, j: (i, jnp.clip(j - 7, 0, 3))),
                   pl.BlockSpec((tm, D), lambda i, j: (i, 0))),
        scratch_shapes=[pltpu.VMEM((tm, D), BF16)],
        compiler_params=_params(dimension_semantics=("parallel", "arbitrary")),
    )(x, g1, w_in_t, b_in)


def _segment_spec(rows, seg, row_index):
    off, n = seg
    if row_index:
        return pl.BlockSpec((rows, TILE), lambda i, j: (i, jnp.clip(j - off, 0, n - 1)))
    return pl.BlockSpec((rows, TILE), lambda j, kk: (kk, jnp.clip(j - off, 0, n - 1)))


def _dh_norm1_bwd(dz_segments, w_in_t, x, g1, dx1):
    tm = 512

    def body(*refs):
        seg_refs = refs[:7]
        w_ref, x_ref, g_ref, dx1_ref, gx_ref, dg_ref, acc = refs[7:]
        i, kk = pl.program_id(0), pl.program_id(1)

        @pl.when(kk == 0)
        def _():
            acc[...] = jnp.zeros_like(acc)

        for s, (off, n) in enumerate(DZ_SEGMENTS):
            @pl.when((kk >= off) & (kk < off + n))
            def _(s=s):
                acc[...] += _dot(seg_refs[s][...], w_ref[...])

        @pl.when((i == 0) & (kk == 0))
        def _():
            dg_ref[...] = jnp.zeros_like(dg_ref)

        @pl.when(kk == N_DZ_TILES - 1)
        def _():
            xv = x_ref[...]
            r = lax.rsqrt(jnp.mean(xv * xv, axis=-1, keepdims=True) + EPS)
            xh = xv * r
            dh = acc[...]
            dg_ref[...] += jnp.sum(dh * xh, axis=0, keepdims=True)
            gx_ref[...] = dx1_ref[...] + _rms_bwd(dh, xh, r, g_ref[...])

    tok = pl.BlockSpec((tm, D), lambda i, j: (i, 0))
    vec = pl.BlockSpec((1, D), lambda i, j: (0, 0))
    return pl.pallas_call(
        body, name="dh_norm1_bwd",
        out_shape=(jax.ShapeDtypeStruct((T, D), F32), jax.ShapeDtypeStruct((1, D), F32)),
        grid=(T // tm, N_DZ_TILES),
        in_specs=[_segment_spec(tm, seg, True) for seg in DZ_SEGMENTS]
        + [pl.BlockSpec((TILE, D), lambda i, j: (j, 0)), tok, vec, tok],
        out_specs=(tok, vec),
        scratch_shapes=[pltpu.VMEM((tm, D), F32)],
        compiler_params=_params(dimension_semantics=("arbitrary", "arbitrary")),
    )(*dz_segments, w_in_t, x, g1, dx1)


def _grad_w_in(dz_segments, h):
    tk = 1024
    nk = T // tk

    def body(*refs):
        seg_refs = refs[:7]
        h_ref, gw_ref, gb_ref, acc = refs[7:]
        j, kk = pl.program_id(0), pl.program_id(1)

        @pl.when(kk == 0)
        def _():
            acc[...] = jnp.zeros_like(acc)
            gb_ref[...] = jnp.zeros_like(gb_ref)

        for s, (off, n) in enumerate(DZ_SEGMENTS):
            @pl.when((j >= off) & (j < off + n))
            def _(s=s):
                a = seg_refs[s][...]
                acc[...] += _dot_tn(a, h_ref[...])
                gb_ref[...] += jnp.sum(a.astype(F32), axis=0, keepdims=True)

        @pl.when(kk == nk - 1)
        def _():
            gw_ref[...] = acc[...].astype(BF16)

    return pl.pallas_call(
        body, name="grad_w_in",
        out_shape=(jax.ShapeDtypeStruct((D_IN, D), BF16), jax.ShapeDtypeStruct((1, D_IN), F32)),
        grid=(N_DZ_TILES, nk),
        in_specs=[_segment_spec(tk, seg, False) for seg in DZ_SEGMENTS]
        + [pl.BlockSpec((tk, D), lambda j, kk: (kk, 0))],
        out_specs=(pl.BlockSpec((TILE, D), lambda j, kk: (j, 0)),
                   pl.BlockSpec((1, TILE), lambda j, kk: (0, j))),
        scratch_shapes=[pltpu.VMEM((TILE, D), F32)],
        compiler_params=_params(dimension_semantics=("parallel", "arbitrary")),
    )(*dz_segments, h)


def _rpb_expand(rpb):
    rpb2 = jnp.pad(rpb.reshape(N_HEADS * N_RPB_R, N_RPB_C), ((0, 0), (0, 128 - N_RPB_C)))
    table = jnp.asarray(_toeplitz_table())

    def body(r_ref, e_ref, o_ref):
        o_ref[...] = _dot_exact(r_ref[...], e_ref[...])

    tb = pl.pallas_call(
        body, name="rpb_expand",
        out_shape=jax.ShapeDtypeStruct((N_HEADS * N_RPB_R, GRID_W * GRID_W), F32),
    )(rpb2, table)
    tb = tb.reshape(N_HEADS, N_RPB_R, GRID_W, GRID_W)
    variants = []
    for oi in range(WIN_H):
        sl = tb[:, WIN_H - 1 - oi: 2 * WIN_H - 1 - oi]
        sl = sl.transpose(0, 2, 1, 3).reshape(N_HEADS // 2, 2 * GRID_W, KWIN)
        variants.append(sl)
    return jnp.stack(variants, axis=0)


def _rpb_reduce(gbias):
    g = gbias.reshape(N_HEADS // 2, WIN_H, 2, GRID_W, WIN_H, GRID_W)
    g = g.transpose(0, 2, 1, 4, 3, 5).reshape(N_HEADS, WIN_H * WIN_H, GRID_W * GRID_W)
    table = jnp.asarray(_toeplitz_table().T.copy())
    shift = jnp.asarray(_row_shift_table())

    def body(g_ref, e_ref, s_ref, o_ref):
        r = _dot_exact(g_ref[0], e_ref[...])
        o_ref[0] = _dot_exact(s_ref[...], r)

    out = pl.pallas_call(
        body, name="rpb_reduce",
        out_shape=jax.ShapeDtypeStruct((N_HEADS, 16, 128), F32),
        grid=(N_HEADS,),
        in_specs=[pl.BlockSpec((1, WIN_H * WIN_H, GRID_W * GRID_W), lambda h: (h, 0, 0)),
                  pl.BlockSpec((GRID_W * GRID_W, 128), lambda h: (0, 0)),
                  pl.BlockSpec((16, WIN_H * WIN_H), lambda h: (0, 0))],
        out_specs=pl.BlockSpec((1, 16, 128), lambda h: (h, 0, 0)),
        compiler_params=_params(dimension_semantics=("arbitrary",)),
    )(g, table, shift)
    return out[:, :N_RPB_R, :N_RPB_C]


def _att_scores(q_ref, k_ref, bias_ref, valid, hmask, r):
    rs = jnp.clip(r - WIN_H // 2, 0, ROWS - WIN_H)
    oi = r - rs
    q0 = pl.multiple_of(r * GRID_W, GRID_W)
    k0 = pl.multiple_of(rs * GRID_W, GRID_W)
    q_r = q_ref[pl.ds(q0, GRID_W), :]
    q2 = jnp.where(hmask, jnp.concatenate([q_r, q_r], axis=0), jnp.zeros((), BF16))
    kw = k_ref[pl.ds(k0, KWIN), :]
    s = _dot_nt(q2, kw) * (DH ** -0.5) + bias_ref[oi, 0]
    s = jnp.where(valid, s, -1e30)
    m = jnp.max(s, axis=-1, keepdims=True)
    p = jnp.exp(s - m)
    p = p / jnp.sum(p, axis=-1, keepdims=True)
    return p, q2, kw, q0, k0, oi


def _att_fwd(qkv, bias):
    valid_np, hmask_np = _att_tables()

    def body(q_ref, k_ref, v_ref, bias_ref, valid_ref, hmask_ref, o_ref):
        valid = valid_ref[...] > 0.5
        hmask = hmask_ref[...] > 0.5
        first_head = lax.broadcasted_iota(jnp.int32, (GRID_W, 2 * DH), 1) < DH

        def row(r, carry):
            p, _, _, q0, k0, _ = _att_scores(q_ref, k_ref, bias_ref, valid, hmask, r)
            o2 = _dot(p.astype(BF16), v_ref[pl.ds(k0, KWIN), :])
            o_ref[pl.ds(q0, GRID_W), :] = jnp.where(first_head, o2[:GRID_W], o2[GRID_W:]).astype(BF16)
            return carry

        lax.fori_loop(0, ROWS, row, 0)

    col = lambda off: pl.BlockSpec((T, 2 * DH), lambda hp: (0, hp + off))
    return pl.pallas_call(
        body, name="att_fwd",
        out_shape=jax.ShapeDtypeStruct((T, D_ATT), BF16),
        grid=(N_HEADS // 2,),
        in_specs=[col(0), col(4), col(8),
                  pl.BlockSpec((WIN_H, 1, 2 * GRID_W, KWIN), lambda hp: (0, hp, 0, 0)),
                  pl.BlockSpec((2 * GRID_W, KWIN), lambda hp: (0, 0)),
                  pl.BlockSpec((2 * DH, 2 * DH), lambda hp: (0, 0))],
        out_specs=pl.BlockSpec((T, 2 * DH), lambda hp: (0, hp)),
        compiler_params=_params(dimension_semantics=("parallel",)),
    )(qkv, qkv, qkv, bias, jnp.asarray(valid_np), jnp.asarray(hmask_np))


def _att_bwd(qkv, bias, datt):
    valid_np, hmask_np = _att_tables()

    def body(q_ref, k_ref, v_ref, do_ref, bias_ref, valid_ref, hmask_ref,
             dq_ref, dk_ref, dv_ref, gb_ref, dk_acc, dv_acc):
        valid = valid_ref[...] > 0.5
        hmask = hmask_ref[...] > 0.5
        first_head = lax.broadcasted_iota(jnp.int32, (GRID_W, 2 * DH), 1) < DH
        dk_acc[...] = jnp.zeros_like(dk_acc)
        dv_acc[...] = jnp.zeros_like(dv_acc)
        gb_ref[...] = jnp.zeros_like(gb_ref)

        def row(r, carry):
            p, q2, kw, q0, k0, oi = _att_scores(q_ref, k_ref, bias_ref, valid, hmask, r)
            do_r = do_ref[pl.ds(q0, GRID_W), :]
            do2 = jnp.where(hmask, jnp.concatenate([do_r, do_r], axis=0), jnp.zeros((), BF16))
            vw = v_ref[pl.ds(k0, KWIN), :]
            dp = _dot_nt(do2, vw)
            ds = p * (dp - jnp.sum(dp * p, axis=-1, keepdims=True))
            p16 = p.astype(BF16)
            ds16 = ds.astype(BF16)
            dv_acc[pl.ds(k0, KWIN), :] += _dot_tn(p16, do2)
            dk_acc[pl.ds(k0, KWIN), :] += _dot_tn(ds16, q2) * (DH ** -0.5)
            dq2 = _dot(ds16, kw) * (DH ** -0.5)
            dq_ref[pl.ds(q0, GRID_W), :] = jnp.where(first_head, dq2[:GRID_W], dq2[GRID_W:]).astype(BF16)
            gb_ref[0, oi] += ds
            return carry

        lax.fori_loop(0, ROWS, row, 0)
        dk_ref[...] = dk_acc[...].astype(BF16)
        dv_ref[...] = dv_acc[...].astype(BF16)

    col = lambda off: pl.BlockSpec((T, 2 * DH), lambda hp: (0, hp + off))
    out_col = pl.BlockSpec((T, 2 * DH), lambda hp: (0, hp))
    return pl.pallas_call(
        body, name="att_bwd",
        out_shape=(jax.ShapeDtypeStruct((T, D_ATT), BF16),) * 3
        + (jax.ShapeDtypeStruct((N_HEADS // 2, WIN_H, 2 * GRID_W, KWIN), F32),),
        grid=(N_HEADS // 2,),
        in_specs=[col(0), col(4), col(8), col(0),
                  pl.BlockSpec((WIN_H, 1, 2 * GRID_W, KWIN), lambda hp: (0, hp, 0, 0)),
                  pl.BlockSpec((2 * GRID_W, KWIN), lambda hp: (0, 0)),
                  pl.BlockSpec((2 * DH, 2 * DH), lambda hp: (0, 0))],
        out_specs=(out_col, out_col, out_col,
                   pl.BlockSpec((1, WIN_H, 2 * GRID_W, KWIN), lambda hp: (hp, 0, 0, 0))),
        scratch_shapes=[pltpu.VMEM((T, 2 * DH), F32), pltpu.VMEM((T, 2 * DH), F32)],
        compiler_params=_params(dimension_semantics=("parallel",)),
    )(qkv, qkv, qkv, datt, bias, jnp.asarray(valid_np), jnp.asarray(hmask_np))


def _conv_taps(up):
    return (_shift_rows(up, 2), _shift_rows(up, 1), up, _shift_rows(up, -1))


def _pair_block_diag(w_pair, dup, same_half):
    return jnp.where(same_half, _dot(w_pair.astype(BF16), dup), 0.0).astype(BF16)


def _gates(u, u16, wa, ba, wi, bi, lam):
    r = _sigmoid(_dot(u16, wa) + ba)
    ig = _sigmoid(_dot(u16, wi) + bi)
    sp = _softplus(-lam)
    log_a = (-LRU_C) * r * sp
    a = jnp.exp(log_a)
    mult = jnp.sqrt(jnp.maximum(-_expm1(2.0 * log_a), 0.0))
    return r, ig, sp, a, mult


def _block_scan(a_ref, b_ref, h_ref, reverse):
    c = a_ref.shape[1]
    nblk = T // 8
    rows = lax.broadcasted_iota(jnp.int32, (8, c), 0)

    def step(i, h_prev):
        blk = (nblk - 1 - i) if reverse else i
        t0 = pl.multiple_of(blk * 8, 8)
        a = a_ref[pl.ds(t0, 8), :]
        b = b_ref[pl.ds(t0, 8), :]
        for s in (1, 2, 4):
            if reverse:
                keep = rows < 8 - s
                a_s = jnp.where(keep, pltpu.roll(a, 8 - s, 0), 1.0)
                b_s = jnp.where(keep, pltpu.roll(b, 8 - s, 0), 0.0)
            else:
                keep = rows >= s
                a_s = jnp.where(keep, pltpu.roll(a, s, 0), 1.0)
                b_s = jnp.where(keep, pltpu.roll(b, s, 0), 0.0)
            b = a * b_s + b
            a = a * a_s
        h = a * h_prev + b
        h_ref[pl.ds(t0, 8), :] = h
        edge = h[0:1] if reverse else h[7:8]
        return jnp.broadcast_to(edge, (8, c))

    lax.fori_loop(0, nblk, step, jnp.zeros((8, c), F32))


def _rec_specs():
    tok = lambda off: pl.BlockSpec((T, CG), lambda g: (0, g + off))
    per_ch = lambda rows: pl.BlockSpec((rows, CG), lambda g: (0, g))
    wspec = pl.BlockSpec((2, 1, CG, REC_BLOCK), lambda g: (0, g, 0, 0))
    const = lambda shape: pl.BlockSpec(shape, lambda g: (0, 0))
    return tok, per_ch, wspec, const


def _rec_fwd(uy, conv_w, conv_b, w_a, b_a, w_i, b_i, lam):
    tok, per_ch, wspec, const = _rec_specs()

    def body(up_ref, yb_ref, cw_ref, cb_ref, wa_ref, ba_ref, wi_ref, bi_ref, lam_ref, dup_ref, half_ref,
             hf_ref, hb_ref, yrec_ref, a_f, bx_f, a_b, bx_b):
        dup = dup_ref[...]
        same_half = half_ref[...] > 0.5
        taps = _conv_taps(up_ref[...])
        u = cb_ref[...]
        for j in range(4):
            u = u + taps[j] * cw_ref[j:j + 1, :]
        u16 = u.astype(BF16)
        for d, (a_s, bx_s) in enumerate(((a_f, bx_f), (a_b, bx_b))):
            wa = _pair_block_diag(wa_ref[d, 0], dup, same_half)
            wi = _pair_block_diag(wi_ref[d, 0], dup, same_half)
            _, ig, _, a, mult = _gates(u, u16, wa, ba_ref[d:d + 1, :], wi, bi_ref[d:d + 1, :],
                                       lam_ref[d:d + 1, :])
            a_s[...] = a
            bx_s[...] = mult * (ig * u)
        _block_scan(a_f, bx_f, hf_ref, False)
        _block_scan(a_b, bx_b, hb_ref, True)
        gelu, _ = _gelu_and_grad(yb_ref[...])
        yrec_ref[...] = ((hf_ref[...] + hb_ref[...]) * gelu).astype(BF16)

    return pl.pallas_call(
        body, name="rec_fwd",
        out_shape=(jax.ShapeDtypeStruct((T, D_REC), F32), jax.ShapeDtypeStruct((T, D_REC), F32),
                   jax.ShapeDtypeStruct((T, D_REC), BF16)),
        grid=(N_CG,),
        in_specs=[tok(0), tok(N_CG), per_ch(4), per_ch(1), wspec, per_ch(2), wspec, per_ch(2), per_ch(2),
                  const((REC_BLOCK, CG)), const((CG, CG))],
        out_specs=(tok(0), tok(0), tok(0)),
        scratch_shapes=[pltpu.VMEM((T, CG), F32)] * 4,
        compiler_params=_params(dimension_semantics=("parallel",)),
    )(uy, uy, conv_w, conv_b, w_a, b_a, w_i, b_i, lam,
      jnp.asarray(_dup_table(), BF16), jnp.asarray(_pair_mask()))


def _rec_bwd(uy, hf, hb, dyrec, conv_w, conv_b, w_a, b_a, w_i, b_i, lam):
    tok, per_ch, wspec, const = _rec_specs()

    def body(up_ref, yb_ref, hf_ref, hb_ref, dy_ref, cw_ref, cb_ref, wa_ref, ba_ref, wi_ref, bi_ref,
             lam_ref, dup_ref, dupt_ref, half_ref,
             dup_out, dyb_ref, dcw_ref, dcb_ref, dwa_ref, dba_ref, dwi_ref, dbi_ref, dlam_ref,
             a_s, dh_s, g_s):
        dup = dup_ref[...]
        dup_t = dupt_ref[...]
        same_half = half_ref[...] > 0.5
        taps = _conv_taps(up_ref[...])
        u = cb_ref[...]
        for j in range(4):
            u = u + taps[j] * cw_ref[j:j + 1, :]
        u16 = u.astype(BF16)
        gelu, dgelu = _gelu_and_grad(yb_ref[...])
        dy = dy_ref[...]
        dyb_ref[...] = (dy * (hf_ref[...] + hb_ref[...]) * dgelu).astype(BF16)
        dh_s[...] = dy * gelu
        du = jnp.zeros((T, CG), F32)
        for d in range(2):
            reverse = d == 1
            wa = _pair_block_diag(wa_ref[d, 0], dup, same_half)
            wi = _pair_block_diag(wi_ref[d, 0], dup, same_half)
            lam_d = lam_ref[d:d + 1, :]
            r, ig, sp, a, mult = _gates(u, u16, wa, ba_ref[d:d + 1, :], wi, bi_ref[d:d + 1, :], lam_d)
            a_s[...] = _shift_rows(a, 1 if reverse else -1)
            _block_scan(a_s, dh_s, g_s, not reverse)
            g = g_s[...]
            h_prev = _shift_rows(hb_ref[...], -1) if reverse else _shift_rows(hf_ref[...], 1)
            da = g * h_prev
            dmult = g * (ig * u)
            dig = g * mult * u
            du = du + g * mult * ig
            dmult_dlog = jnp.where(mult > 0.0, -(a * a) / mult, 0.0)
            dlog_a = da * a + dmult * dmult_dlog
            dr = dlog_a * ((-LRU_C) * sp)
            dsp = jnp.sum(dlog_a * ((-LRU_C) * r), axis=0, keepdims=True)
            dlam_ref[d:d + 1, :] = dsp * (-_sigmoid(-lam_d))
            dga = dr * r * (1.0 - r)
            dgi = dig * ig * (1.0 - ig)
            dga16 = dga.astype(BF16)
            dgi16 = dgi.astype(BF16)
            du = du + _dot_nt(dga16, wa) + _dot_nt(dgi16, wi)
            dwa_ref[d, 0] = _dot_exact(jnp.where(same_half, _dot_tn(u16, dga16), 0.0), dup_t)
            dwi_ref[d, 0] = _dot_exact(jnp.where(same_half, _dot_tn(u16, dgi16), 0.0), dup_t)
            dba_ref[d:d + 1, :] = jnp.sum(dga, axis=0, keepdims=True)
            dbi_ref[d:d + 1, :] = jnp.sum(dgi, axis=0, keepdims=True)
        dcb_ref[...] = jnp.sum(du, axis=0, keepdims=True)
        for j in range(4):
            dcw_ref[j:j + 1, :] = jnp.sum(du * taps[j], axis=0, keepdims=True)
        dup_in = (_shift_rows(du, -2) * cw_ref[0:1, :] + _shift_rows(du, -1) * cw_ref[1:2, :]
                  + du * cw_ref[2:3, :] + _shift_rows(du, 1) * cw_ref[3:4, :])
        dup_out[...] = dup_in.astype(BF16)

    wshape = jax.ShapeDtypeStruct((2, N_CG, CG, REC_BLOCK), F32)
    vec = lambda rows: jax.ShapeDtypeStruct((rows, D_REC), F32)
    dup_np = _dup_table()
    return pl.pallas_call(
        body, name="rec_bwd",
        out_shape=(jax.ShapeDtypeStruct((T, D_REC), BF16), jax.ShapeDtypeStruct((T, D_REC), BF16),
                   vec(4), vec(1), wshape, vec(2), wshape, vec(2), vec(2)),
        grid=(N_CG,),
        in_specs=[tok(0), tok(N_CG), tok(0), tok(0), tok(0),
                  per_ch(4), per_ch(1), wspec, per_ch(2), wspec, per_ch(2), per_ch(2),
                  const((REC_BLOCK, CG)), const((CG, REC_BLOCK)), const((CG, CG))],
        out_specs=(tok(0), tok(0), per_ch(4), per_ch(1), wspec, per_ch(2), wspec, per_ch(2), per_ch(2)),
        scratch_shapes=[pltpu.VMEM((T, CG), F32)] * 3,
        compiler_params=_params(dimension_semantics=("parallel",)),
    )(uy, uy, hf, hb, dyrec, conv_w, conv_b, w_a, b_a, w_i, b_i, lam,
      jnp.asarray(dup_np, BF16), jnp.asarray(dup_np.T.copy()), jnp.asarray(_pair_mask()))


TM_MIX = 256


def _mix_specs():
    tok = lambda width, blk=0: pl.BlockSpec((TM_MIX, width), lambda i: (i, blk))
    full = lambda shape: pl.BlockSpec(shape, lambda i: (0, 0))
    return tok, full


def _mix_fwd(x, att, yrec, gg, w_att_o_t, w_rec_o, w_out):
    tok, full = _mix_specs()

    def body(x_ref, att_ref, yr_ref, ga_ref, gr_ref, wao_ref, wro_ref, wo_ref, x1_ref, mixed_ref):
        y_att = _dot_nt(att_ref[...], wao_ref[...])
        y_rec = _dot(yr_ref[...], wro_ref[...])
        mixed = (_sigmoid(ga_ref[...]) * y_att + _sigmoid(gr_ref[...]) * y_rec).astype(BF16)
        mixed_ref[...] = mixed
        x1_ref[...] = x_ref[...] + _dot(mixed, wo_ref[...])

    return pl.pallas_call(
        body, name="mix_fwd",
        out_shape=(jax.ShapeDtypeStruct((T, D), F32), jax.ShapeDtypeStruct((T, D), BF16)),
        grid=(T // TM_MIX,),
        in_specs=[tok(D), tok(D_ATT), tok(D_REC), tok(D, 0), tok(D, 1),
                  full((D, D_ATT)), full((D_REC, D)), full((D, D))],
        out_specs=(tok(D), tok(D)),
        compiler_params=_params(dimension_semantics=("parallel",)),
    )(x, att, yrec, gg, gg, w_att_o_t, w_rec_o, w_out)


def _mix_bwd(dx1, att, yrec, gg, w_att_o_t, w_rec_o, w_out):
    tok, full = _mix_specs()

    def body(dx_ref, att_ref, yr_ref, ga_ref, gr_ref, wao_ref, wro_ref, wo_ref,
             dga_ref, dgr_ref, dya_ref, dyr_ref, datt_ref, dyrp_ref):
        dmixed = _dot_nt(dx_ref[...].astype(BF16), wo_ref[...])
        y_att = _dot_nt(att_ref[...], wao_ref[...])
        y_rec = _dot(yr_ref[...], wro_ref[...])
        sa = _sigmoid(ga_ref[...])
        sr = _sigmoid(gr_ref[...])
        dga_ref[...] = (dmixed * y_att * sa * (1.0 - sa)).astype(BF16)
        dgr_ref[...] = (dmixed * y_rec * sr * (1.0 - sr)).astype(BF16)
        dya = (dmixed * sa).astype(BF16)
        dyr = (dmixed * sr).astype(BF16)
        dya_ref[...] = dya
        dyr_ref[...] = dyr
        datt_ref[...] = _dot(dya, wao_ref[...]).astype(BF16)
        dyrp_ref[...] = _dot_nt(dyr, wro_ref[...])

    return pl.pallas_call(
        body, name="mix_bwd",
        out_shape=(jax.ShapeDtypeStruct((T, D), BF16), jax.ShapeDtypeStruct((T, D), BF16),
                   jax.ShapeDtypeStruct((T, D), BF16), jax.ShapeDtypeStruct((T, D), BF16),
                   jax.ShapeDtypeStruct((T, D_ATT), BF16), jax.ShapeDtypeStruct((T, D_REC), F32)),
        grid=(T // TM_MIX,),
        in_specs=[tok(D), tok(D_ATT), tok(D_REC), tok(D, 0), tok(D, 1),
                  full((D, D_ATT)), full((D_REC, D)), full((D, D))],
        out_specs=(tok(D), tok(D), tok(D), tok(D), tok(D_ATT), tok(D_REC)),
        compiler_params=_params(dimension_semantics=("parallel",)),
    )(dx1, att, yrec, gg, gg, w_att_o_t, w_rec_o, w_out)


TM_FFN = 256
FF_CHUNK = 1024


def _ffn_loss(x1, target, g2, gf, w_ff1_t, w_ff2):
    n_chunks = D_FF // FF_CHUNK

    def body(x1_ref, tg_ref, g2_ref, gf_ref, w1_hbm, w2_hbm,
             loss_ref, dx1_ref, h2_ref, act_ref, dpre_ref, dx2_ref, dg2_ref, dgf_ref,
             w1, w2, relu_s):
        i = pl.program_id(0)

        @pl.when(i == 0)
        def _():
            pltpu.sync_copy(w1_hbm, w1)
            pltpu.sync_copy(w2_hbm, w2)
            loss_ref[...] = jnp.zeros_like(loss_ref)
            dg2_ref[...] = jnp.zeros_like(dg2_ref)
            dgf_ref[...] = jnp.zeros_like(dgf_ref)

        x1v = x1_ref[...]
        r2 = lax.rsqrt(jnp.mean(x1v * x1v, axis=-1, keepdims=True) + EPS)
        xh2 = x1v * r2
        h2 = (xh2 * g2_ref[...]).astype(BF16)
        h2_ref[...] = h2
        x2 = x1v
        for c in range(n_chunks):
            ff = slice(c * FF_CHUNK, (c + 1) * FF_CHUNK)
            rl = jnp.maximum(_dot_nt(h2, w1[ff, :]), 0.0)
            relu_s[:, ff] = rl
            act = (rl * rl).astype(BF16)
            act_ref[:, ff] = act
            x2 = x2 + _dot(act, w2[ff, :])
        r3 = lax.rsqrt(jnp.mean(x2 * x2, axis=-1, keepdims=True) + EPS)
        xh3 = x2 * r3
        err = xh3 * gf_ref[...] - tg_ref[...]
        loss_ref[...] += 0.5 * jnp.sum(jnp.mean(err * err, axis=-1, keepdims=True))
        dy = err * (1.0 / D)
        dgf_ref[...] += jnp.sum(dy * xh3, axis=0, keepdims=True)
        dx2 = _rms_bwd(dy, xh3, r3, gf_ref[...])
        dx2_16 = dx2.astype(BF16)
        dx2_ref[...] = dx2_16
        dh2 = jnp.zeros((TM_FFN, D), F32)
        for c in range(n_chunks):
            ff = slice(c * FF_CHUNK, (c + 1) * FF_CHUNK)
            dpre = (_dot_nt(dx2_16, w2[ff, :]) * (2.0 * relu_s[:, ff])).astype(BF16)
            dpre_ref[:, ff] = dpre
            dh2 = dh2 + _dot(dpre, w1[ff, :])
        dg2_ref[...] += jnp.sum(dh2 * xh2, axis=0, keepdims=True)
        dx1_ref[...] = dx2 + _rms_bwd(dh2, xh2, r2, g2_ref[...])

    tok = lambda width: pl.BlockSpec((TM_FFN, width), lambda i: (i, 0))
    vec = pl.BlockSpec((1, D), lambda i: (0, 0))
    hbm = pl.BlockSpec(memory_space=pl.ANY)
    return pl.pallas_call(
        body, name="ffn_loss",
        out_shape=(jax.ShapeDtypeStruct((8, 128), F32), jax.ShapeDtypeStruct((T, D), F32),
                   jax.ShapeDtypeStruct((T, D), BF16), jax.ShapeDtypeStruct((T, D_FF), BF16),
                   jax.ShapeDtypeStruct((T, D_FF), BF16), jax.ShapeDtypeStruct((T, D), BF16),
                   jax.ShapeDtypeStruct((1, D), F32), jax.ShapeDtypeStruct((1, D), F32)),
        grid=(T // TM_FFN,),
        in_specs=[tok(D), tok(D), vec, vec, hbm, hbm],
        out_specs=(pl.BlockSpec((8, 128), lambda i: (0, 0)), tok(D), tok(D), tok(D_FF), tok(D_FF), tok(D),
                   vec, vec),
        scratch_shapes=[pltpu.VMEM((D_FF, D), BF16), pltpu.VMEM((D_FF, D), BF16),
                        pltpu.VMEM((TM_FFN, D_FF), F32)],
        compiler_params=_params(dimension_semantics=("arbitrary",)),
    )(x1, target, g2, gf, w_ff1_t, w_ff2)


def _local_step(x, target, p):
    bias = _rpb_expand(p["rpb"])
    pairs = lambda w: w.reshape(2, N_CG, CG, REC_BLOCK)
    w_a, w_i = pairs(p["w_rg_a"]), pairs(p["w_rg_i"])
    rec_params = (p["conv_w"], p["conv_b"], w_a, p["b_rg_a"], w_i, p["b_rg_i"], p["lru_lambda"])

    qkv, uy, gg, h = _in_proj(x, p["ln1_g"], p["w_in_t"], p["b_in"])
    att = _att_fwd(qkv, bias)
    hf, hb, yrec = _rec_fwd(uy, *rec_params)
    x1, mixed = _mix_fwd(x, att, yrec, gg, p["w_att_o_t"], p["w_rec_o"], p["w_out"])
    loss8, dx1, h2, act, dpre, dx2, g_ln2, g_lnf = _ffn_loss(
        x1, target, p["ln2_g"], p["lnf_g"], p["w_ff1_t"], p["w_ff2"])

    dga, dgr, dya, dyr, datt, dyrp = _mix_bwd(dx1, att, yrec, gg, p["w_att_o_t"], p["w_rec_o"], p["w_out"])
    dup, dyb, g_cw, g_cb, g_wa, g_ba, g_wi, g_bi, g_lam = _rec_bwd(uy, hf, hb, dyrp, *rec_params)
    dq, dk, dv, gbias = _att_bwd(qkv, bias, datt)
    dz = (dq, dk, dv, dup, dyb, dga, dgr)
    grad_x, g_ln1 = _dh_norm1_bwd(dz, p["w_in_t"], x, p["ln1_g"], dx1)
    g_w_in_t, g_b_in = _grad_w_in(dz, h)

    blocks = lambda g: g.reshape(2, N_REC_BLOCKS, REC_BLOCK, REC_BLOCK)
    grads = {
        "ln1_g": g_ln1, "w_in_t": g_w_in_t, "b_in": g_b_in, "rpb": _rpb_reduce(gbias),
        "w_att_o_t": _matmul(dya, att, "tn", BF16, "g_w_att_o"),
        "conv_w": g_cw, "conv_b": g_cb, "w_rg_a": blocks(g_wa), "b_rg_a": g_ba,
        "w_rg_i": blocks(g_wi), "b_rg_i": g_bi, "lru_lambda": g_lam,
        "w_rec_o": _matmul(yrec, dyr, "tn", BF16, "g_w_rec_o"),
        "w_out": _matmul(mixed, dx1, "tn", BF16, "g_w_out"),
        "ln2_g": g_ln2,
        "w_ff1_t": _matmul(dpre, h2, "tn", BF16, "g_w_ff1"),
        "w_ff2": _matmul(act, dx2, "tn", BF16, "g_w_ff2"),
        "lnf_g": g_lnf,
    }
    return loss8[0:1, 0:1], grad_x, grads


MESH_ID = pl.DeviceIdType.MESH
ANY = pl.BlockSpec(memory_space=pl.ANY)

CHAN_BLOCK_ROWS = 32
SECTIONS = (("w_in_t", 704, D), ("w_rec_o", 128, D), ("w_out", 128, D), ("w_ff1_t", 512, D),
            ("w_ff2", 512, D), ("chan", CHAN_BLOCK_ROWS, D), ("w_att_o_t", 128, D_ATT))
N_SEC = len(SECTIONS)
N_CHAN_ROWS = 10
CHAN = (("conv_w", 4), ("b_rg_a", 2), ("b_rg_i", 2), ("lru_lambda", 2))


def _position():
    return lax.axis_index("x"), lax.axis_index("y"), lax.axis_index("c")


def _other_chips(x, y):
    return [(1 - x, y), (x, 1 - y), (1 - x, 1 - y)]


def _block_of(ref, dev, rows):
    return ref.at[pl.ds(pl.multiple_of(dev * rows, 16), rows)]


def _all_gather(shards, name):
    ns = len(shards)

    def body(*refs):
        x_refs, out_refs = refs[:ns], refs[ns:2 * ns]
        send_sems, recv_sems, local_sems = refs[2 * ns:]
        x, y, c = _position()
        me, sibling = (x, y, c), (x, y, 1 - c)
        chips = _other_chips(x, y)

        def rows(s, px, py, pc):
            return _block_of(out_refs[s], 4 * px + 2 * py + pc, shards[s].shape[0])

        def copy(k, s, block, to, from_shard=False):
            return pltpu.make_async_remote_copy(
                src_ref=x_refs[s] if from_shard else rows(s, *block), dst_ref=rows(s, *block),
                send_sem=send_sems.at[k * ns + s], recv_sem=recv_sems.at[k * ns + s],
                device_id=to, device_id_type=MESH_ID)

        sections = range(ns)
        mine = [pltpu.make_async_copy(x_refs[s], rows(s, *me), local_sems.at[s]) for s in sections]
        first = [copy(0, s, me, sibling, True) for s in sections]
        first += [copy(1 + j, s, me, (*chip, c), True) for j, chip in enumerate(chips) for s in sections]
        for cp in mine + first:
            cp.start()
        passed = []
        for j, chip in enumerate(chips):
            for s in sections:
                copy(1 + j, s, (*chip, c), me).wait_recv()
                passed.append(copy(4 + j, s, (*chip, c), sibling))
                passed[-1].start()
        for s in sections:
            copy(0, s, sibling, me).wait_recv()
        for j, chip in enumerate(chips):
            for s in sections:
                copy(4 + j, s, (*chip, 1 - c), me).wait_recv()
        for cp in first + passed:
            cp.wait_send()
        for cp in mine:
            cp.wait()

    return pl.pallas_call(
        body, name=name,
        out_shape=tuple(jax.ShapeDtypeStruct((N_DEV * s.shape[0], s.shape[1]), s.dtype) for s in shards),
        in_specs=[ANY] * ns, out_specs=(ANY,) * ns,
        scratch_shapes=[pltpu.SemaphoreType.DMA((7 * ns,)), pltpu.SemaphoreType.DMA((7 * ns,)),
                        pltpu.SemaphoreType.DMA((ns,))],
    )(*shards)


def _pair_exchange(grads):
    def body(*refs):
        g_refs, land = refs[:N_SEC], refs[N_SEC:2 * N_SEC]
        send_sems, recv_sems = refs[2 * N_SEC:]
        x, y, c = _position()
        copies = [pltpu.make_async_remote_copy(
            src_ref=_block_of(g_refs[s], 2 * k + 1 - c, rows), dst_ref=land[s].at[k],
            send_sem=send_sems.at[k * N_SEC + s], recv_sem=recv_sems.at[k * N_SEC + s],
            device_id=(x, y, 1 - c), device_id_type=MESH_ID)
            for k in range(N_CHIPS) for s, (_, rows, _) in enumerate(SECTIONS)]
        for cp in copies:
            cp.start()
        for cp in copies:
            cp.wait_recv()
        for cp in copies:
            cp.wait_send()

    n = N_CHIPS * N_SEC
    return pl.pallas_call(
        body, name="grad_pair_exchange",
        out_shape=tuple(jax.ShapeDtypeStruct((N_CHIPS, rows, cols), BF16) for _, rows, cols in SECTIONS),
        in_specs=[ANY] * N_SEC, out_specs=(ANY,) * N_SEC,
        scratch_shapes=[pltpu.SemaphoreType.DMA((n,)), pltpu.SemaphoreType.DMA((n,))],
    )(*grads)


def _pair_add(grads, got, core):
    def body(core_ref, *refs):
        g_refs, got_refs, p_refs = refs[:N_SEC], refs[N_SEC:2 * N_SEC], refs[2 * N_SEC:]
        for s in range(N_SEC):
            p_refs[s][0] = (g_refs[s][...].astype(F32) + got_refs[s][0].astype(F32)).astype(BF16)

    slot = [pl.BlockSpec((1, rows, cols), lambda k, c: (k, 0, 0)) for _, rows, cols in SECTIONS]
    return pl.pallas_call(
        body, name="grad_pair_add",
        out_shape=tuple(jax.ShapeDtypeStruct((N_CHIPS, rows, cols), BF16) for _, rows, cols in SECTIONS),
        grid_spec=pltpu.PrefetchScalarGridSpec(
            num_scalar_prefetch=1, grid=(N_CHIPS,),
            in_specs=[pl.BlockSpec((rows, cols), lambda k, c: (2 * k + c[0], 0)) for _, rows, cols in SECTIONS]
            + slot,
            out_specs=tuple(slot)),
        compiler_params=_params(dimension_semantics=("parallel",)),
    )(core, *grads, *got)


def _chip_exchange(parts):
    def body(*refs):
        p_refs, land = refs[:N_SEC], refs[N_SEC:2 * N_SEC]
        send_sems, recv_sems = refs[2 * N_SEC:]
        x, y, c = _position()
        copies = [pltpu.make_async_remote_copy(
            src_ref=p_refs[s].at[2 * cx + cy], dst_ref=land[s].at[j],
            send_sem=send_sems.at[j * N_SEC + s], recv_sem=recv_sems.at[j * N_SEC + s],
            device_id=(cx, cy, c), device_id_type=MESH_ID)
            for j, (cx, cy) in enumerate(_other_chips(x, y)) for s in range(N_SEC)]
        for cp in copies:
            cp.start()
        for cp in copies:
            cp.wait_recv()
        for cp in copies:
            cp.wait_send()

    n = 3 * N_SEC
    return pl.pallas_call(
        body, name="grad_chip_exchange",
        out_shape=tuple(jax.ShapeDtypeStruct((3, rows, cols), BF16) for _, rows, cols in SECTIONS),
        in_specs=[ANY] * N_SEC, out_specs=(ANY,) * N_SEC,
        scratch_shapes=[pltpu.SemaphoreType.DMA((n,)), pltpu.SemaphoreType.DMA((n,))],
    )(*parts)


def _grad_finish(parts, far, chip):
    def body(chip_ref, *refs):
        p_refs, b_refs, g_refs = refs[:N_SEC], refs[N_SEC:2 * N_SEC], refs[2 * N_SEC:]
        for s in range(N_SEC):
            g = p_refs[s][0].astype(F32)
            for j in range(3):
                g = g + b_refs[s][j].astype(F32)
            g_refs[s][...] = g

    half = [(rows // 2, cols) for _, rows, cols in SECTIONS]
    return pl.pallas_call(
        body, name="grad_finish",
        out_shape=tuple(jax.ShapeDtypeStruct((rows, cols), F32) for _, rows, cols in SECTIONS),
        grid_spec=pltpu.PrefetchScalarGridSpec(
            num_scalar_prefetch=1, grid=(2,),
            in_specs=[pl.BlockSpec((1, r, c), lambda i, chip: (chip[0], i, 0)) for r, c in half]
            + [pl.BlockSpec((3, r, c), lambda i, chip: (0, i, 0)) for r, c in half],
            out_specs=tuple(pl.BlockSpec((r, c), lambda i, chip: (i, 0)) for r, c in half)),
        compiler_params=_params(dimension_semantics=("parallel",)),
    )(chip, *parts, *far)


def _sum_devices(parts, rows):
    tr = rows // 2

    def body(*refs):
        s = refs[0][...]
        for d in range(1, N_DEV):
            s = s + refs[d][...]
        refs[N_DEV][...] = s

    return pl.pallas_call(
        body, name="small_grad_sum",
        out_shape=jax.ShapeDtypeStruct((rows, LANES), F32),
        grid=(2,),
        in_specs=[pl.BlockSpec((tr, LANES), lambda i, d=d: (2 * d + i, 0)) for d in range(N_DEV)],
        out_specs=pl.BlockSpec((tr, LANES), lambda i: (i, 0)),
        compiler_params=_params(dimension_semantics=("parallel",)),
    )(*([parts] * N_DEV))


def _adamw(w, g, m, v, name):
    rows, cols = w.shape
    tr = rows
    while tr * cols * 4 > (1 << 20) and tr % 16 == 0:
        tr //= 2
    c1 = 1.0 / (1.0 - ADAM_B1 ** ADAM_STEP)
    c2 = 1.0 / (1.0 - ADAM_B2 ** ADAM_STEP)

    def body(w_ref, g_ref, m_ref, v_ref, d_ref, nm_ref, nv_ref):
        gv = g_ref[...]
        nm = ADAM_B1 * m_ref[...] + (1.0 - ADAM_B1) * gv
        nv = ADAM_B2 * v_ref[...] + (1.0 - ADAM_B2) * (gv * gv)
        nm_ref[...] = nm
        nv_ref[...] = nv
        d_ref[...] = (-ADAM_LR) * ((nm * c1) / (jnp.sqrt(nv * c2) + ADAM_EPS) + ADAM_WD * w_ref[...])

    spec = pl.BlockSpec((tr, cols), lambda i: (i, 0))
    shape = jax.ShapeDtypeStruct((rows, cols), F32)
    return pl.pallas_call(
        body, name=name,
        out_shape=(shape, shape, shape),
        grid=(rows // tr,),
        in_specs=[spec] * 4, out_specs=(spec,) * 3,
        compiler_params=_params(dimension_semantics=("parallel",)),
    )(w, g, m, v)


NAMES = ("ln1_g", "w_in", "b_in", "rpb", "w_att_o", "conv_w", "conv_b", "w_rg_a", "b_rg_a", "w_rg_i",
         "b_rg_i", "lru_lambda", "w_rec_o", "w_out", "ln2_g", "w_ff1", "w_ff2", "lnf_g")
TRANSPOSED = {"w_in": "w_in_t", "w_att_o": "w_att_o_t", "w_ff1": "w_ff1_t"}
ROW_SHARDED = ("w_rec_o", "w_out", "w_ff2")
REPLICATED = (("ln1_g", (1, D)), ("b_in", (1, D_IN)), ("rpb", (N_HEADS * N_RPB_R, N_RPB_C)),
              ("conv_b", (1, D_REC)), ("w_rg_a", (2 * N_REC_BLOCKS * REC_BLOCK, REC_BLOCK)),
              ("w_rg_i", (2 * N_REC_BLOCKS * REC_BLOCK, REC_BLOCK)), ("ln2_g", (1, D)), ("lnf_g", (1, D)))
SMALL_ROWS = 2160


def _chan_bits(vectors):
    chan = jnp.concatenate(vectors, axis=0)
    bits = lax.bitcast_convert_type(chan, BF16).reshape(-1)
    return jnp.pad(bits, (0, CHAN_BLOCK_ROWS * D - bits.shape[0])).reshape(CHAN_BLOCK_ROWS, D)


def _chan_from_bits(gathered):
    bits = gathered.reshape(N_DEV, CHAN_BLOCK_ROWS * D)[:, :2 * N_CHAN_ROWS * LANES]
    chan = lax.bitcast_convert_type(bits.reshape(N_DEV, N_CHAN_ROWS, LANES, 2), F32)
    return chan.transpose(1, 0, 2).reshape(N_CHAN_ROWS, D)


def kernel(x, ln1_g, w_in, b_in, rpb, w_att_o, conv_w, conv_b, w_rg_a, b_rg_a, w_rg_i, b_rg_i, lru_lambda, w_rec_o, w_out, ln2_g, w_ff1, w_ff2, lnf_g, loss_target, m_ln1_g, m_w_in, m_b_in, m_rpb, m_w_att_o, m_conv_w, m_conv_b, m_w_rg_a, m_b_rg_a, m_w_rg_i, m_b_rg_i, m_lru_lambda, m_w_rec_o, m_w_out, m_ln2_g, m_w_ff1, m_w_ff2, m_lnf_g, v_ln1_g, v_w_in, v_b_in, v_rpb, v_w_att_o, v_conv_w, v_conv_b, v_w_rg_a, v_b_rg_a, v_w_rg_i, v_b_rg_i, v_lru_lambda, v_w_rec_o, v_w_out, v_ln2_g, v_w_ff1, v_w_ff2, v_lnf_g):
    w = dict(zip(NAMES, (ln1_g, w_in, b_in, rpb, w_att_o, conv_w, conv_b, w_rg_a, b_rg_a, w_rg_i,
                         b_rg_i, lru_lambda, w_rec_o, w_out, ln2_g, w_ff1, w_ff2, lnf_g)))
    m = dict(zip(NAMES, (m_ln1_g, m_w_in, m_b_in, m_rpb, m_w_att_o, m_conv_w, m_conv_b, m_w_rg_a,
                         m_b_rg_a, m_w_rg_i, m_b_rg_i, m_lru_lambda, m_w_rec_o, m_w_out, m_ln2_g,
                         m_w_ff1, m_w_ff2, m_lnf_g)))
    v = dict(zip(NAMES, (v_ln1_g, v_w_in, v_b_in, v_rpb, v_w_att_o, v_conv_w, v_conv_b, v_w_rg_a,
                         v_b_rg_a, v_w_rg_i, v_b_rg_i, v_lru_lambda, v_w_rec_o, v_w_out, v_ln2_g,
                         v_w_ff1, v_w_ff2, v_lnf_g)))
    xi, yi, ci = _position()

    shard = {t: w[n][0].T.astype(BF16) for n, t in TRANSPOSED.items()}
    shard.update({n: w[n][0].astype(BF16) for n in ROW_SHARDED})
    shard["chan"] = _chan_bits([w[n][0] for n, _ in CHAN])
    gathered = dict(zip((n for n, _, _ in SECTIONS),
                        _all_gather([shard[n] for n, _, _ in SECTIONS], "weight_all_gather")))
    chan = _chan_from_bits(gathered.pop("chan"))
    p = dict(gathered)
    r0 = 0
    for n, rows in CHAN:
        p[n] = chan[r0:r0 + rows]
        r0 += rows
    p.update(ln1_g=w["ln1_g"], b_in=w["b_in"], rpb=w["rpb"][0], conv_b=w["conv_b"],
             w_rg_a=w["w_rg_a"][0], w_rg_i=w["w_rg_i"][0], ln2_g=w["ln2_g"],
             lnf_g=w["lnf_g"].reshape(1, D))

    loss_part, grad_x, grads = _local_step(x[0], loss_target[0], p)

    chan_g = jnp.concatenate([grads[n] for n, _ in CHAN], axis=0)
    chan_g = chan_g.reshape(N_CHAN_ROWS, N_DEV, LANES).transpose(1, 0, 2).astype(BF16)
    chan_g = jnp.pad(chan_g.reshape(N_DEV, -1), ((0, 0), (0, CHAN_BLOCK_ROWS * D - N_CHAN_ROWS * LANES)))
    grads["chan"] = chan_g.reshape(N_DEV * CHAN_BLOCK_ROWS, D)
    sect = [grads[n] for n, _, _ in SECTIONS]
    core = jnp.reshape(ci, (1,)).astype(jnp.int32)
    chip = jnp.reshape(2 * xi + yi, (1,)).astype(jnp.int32)
    parts = _pair_add(sect, _pair_exchange(sect), core)
    summed = dict(zip((n for n, _, _ in SECTIONS), _grad_finish(parts, _chip_exchange(parts), chip)))

    flat = jnp.concatenate([grads[n].reshape(-1) for n, _ in REPLICATED] + [loss_part.reshape(-1)])
    n_small = flat.shape[0]
    flat = jnp.pad(flat, (0, SMALL_ROWS * LANES - n_small)).reshape(SMALL_ROWS, LANES)
    (small_parts,) = _all_gather([flat], "small_grad_all_gather")
    small = _sum_devices(small_parts, SMALL_ROWS).reshape(-1)
    loss = small[n_small - 1]

    g, delta, new_m, new_v = {}, {}, {}, {}

    def update(n, g2, shape2):
        d2, m2, v2 = _adamw(w[n].reshape(shape2), g2, m[n].reshape(shape2), v[n].reshape(shape2),
                            "adamw_" + n)
        g[n], delta[n], new_m[n], new_v[n] = (a.reshape(w[n].shape) for a in (g2, d2, m2, v2))

    for n in ROW_SHARDED:
        update(n, summed[n], summed[n].shape)
    for n, t in TRANSPOSED.items():
        update(n, summed[t].T, summed[t].shape[::-1])
    chan_back = summed["chan"].reshape(-1)[:N_CHAN_ROWS * LANES].reshape(N_CHAN_ROWS, LANES)
    r0 = 0
    for n, rows in CHAN:
        update(n, chan_back[r0:r0 + rows], (rows, LANES))
        r0 += rows
    o = 0
    for n, shape2 in REPLICATED:
        size = shape2[0] * shape2[1]
        update(n, small[o:o + size].reshape(shape2), shape2)
        o += size

    return (loss, grad_x[None], *[g[n] for n in NAMES], *[delta[n] for n in NAMES],
            *[new_m[n] for n in NAMES], *[new_v[n] for n in NAMES])
```

```python
import math

import numpy as np
import jax
import jax.numpy as jnp
from jax import lax
from jax.experimental import pallas as pl
from jax.experimental.pallas import tpu as pltpu

F32 = jnp.float32
BF16 = jnp.bfloat16

T = 2048
D = 1024
D_ATT = 512
D_REC = 1024
D_FF = 4096
D_IN = 5632
N_HEADS = 8
DH = 64
GRID_W = 64
ROWS = T // GRID_W
WIN_H = 8
WIN_W = 16
KWIN = WIN_H * GRID_W
N_RPB_R = 2 * WIN_H - 1
N_RPB_C = 2 * WIN_W - 1
N_REC_BLOCKS = 16
REC_BLOCK = 64
CG = 128
N_CG = D_REC // CG
LRU_C = 8.0
EPS = 1e-6
N_DEV = 8
N_CHIPS = 4
LANES = 128

ADAM_LR = 0.001
ADAM_B1 = 0.9
ADAM_B2 = 0.999
ADAM_EPS = 1e-08
ADAM_WD = 0.01
ADAM_STEP = 10

MESH_AXES = ("x", "y", "c")
VMEM_LIMIT = 56 * 1024 * 1024

TILE = 512
DZ_SEGMENTS = ((0, 1), (1, 1), (2, 1), (3, 2), (5, 2), (7, 2), (9, 2))
N_DZ_TILES = D_IN // TILE


def _params(**kw):
    return pltpu.CompilerParams(vmem_limit_bytes=VMEM_LIMIT, **kw)


def _att_tables():
    rq = np.arange(2 * GRID_W) % GRID_W
    kc = np.arange(KWIN) % GRID_W
    win_start = np.clip(rq - WIN_W // 2, 0, GRID_W - WIN_W)
    valid = (kc[None, :] >= win_start[:, None]) & (kc[None, :] < win_start[:, None] + WIN_W)
    return valid.astype(np.float32), _pair_mask()


def _pair_mask():
    half = np.arange(2 * DH) // DH
    return (half[:, None] == half[None, :]).astype(np.float32)


def _dup_table():
    return np.concatenate([np.eye(REC_BLOCK, dtype=np.float32)] * 2, axis=1)


def _toeplitz_table():
    q = np.arange(GRID_W)[:, None]
    kc = np.arange(GRID_W)[None, :]
    dc = (kc - q + WIN_W - 1).reshape(-1)
    e = np.zeros((128, GRID_W * GRID_W), np.float32)
    ok = (dc >= 0) & (dc < N_RPB_C)
    e[dc[ok], np.arange(GRID_W * GRID_W)[ok]] = 1.0
    return e


def _row_shift_table():
    s = np.zeros((16, WIN_H * WIN_H), np.float32)
    for oi in range(WIN_H):
        for i in range(WIN_H):
            s[i - oi + WIN_H - 1, oi * WIN_H + i] = 1.0
    return s


def _sigmoid(x):
    return 0.5 * jnp.tanh(0.5 * x) + 0.5


def _softplus(x):
    return jnp.maximum(x, 0.0) + jnp.log(1.0 + jnp.exp(-jnp.abs(x)))


def _one_minus_square(log_a, a):
    x = 2.0 * log_a
    series = -x * (1.0 + x * (0.5 + x * (1.0 / 6.0)))
    return jnp.where(x > -0.02, series, 1.0 - a * a)


_GELU_C = math.sqrt(2.0 / math.pi)


def _gelu_and_grad(x):
    x2 = x * x
    inner = _GELU_C * (x + 0.044715 * x * x2)
    t = jnp.tanh(inner)
    g = 0.5 * x * (1.0 + t)
    dg = 0.5 * (1.0 + t) + 0.5 * x * (1.0 - t * t) * _GELU_C * (1.0 + 3.0 * 0.044715 * x2)
    return g, dg


def _dot(a, b):
    return jnp.dot(a, b, preferred_element_type=F32)


def _dot_nt(a, b):
    return lax.dot_general(a, b, (((1,), (1,)), ((), ())), preferred_element_type=F32)


def _dot_tn(a, b):
    return lax.dot_general(a, b, (((0,), (0,)), ((), ())), preferred_element_type=F32)


def _dot_exact(a, b):
    return jnp.dot(a, b, precision=lax.Precision.HIGHEST, preferred_element_type=F32)


def _shift_rows(x, s):
    n = x.shape[0]
    rows = lax.broadcasted_iota(jnp.int32, x.shape, 0)
    y = pltpu.roll(x, s % n, 0)
    if s > 0:
        return jnp.where(rows >= s, y, 0.0)
    return jnp.where(rows < n + s, y, 0.0)


def _rms_bwd(dh, xh, r, g):
    dxh = dh * g
    return r * (dxh - xh * jnp.mean(dxh * xh, axis=-1, keepdims=True))


def _matmul(a, b, mode, out_dtype, name, tm=512, tn=1024, tk=2048):
    if mode == "nn":
        (m, k), (k2, n) = a.shape, b.shape
    elif mode == "nt":
        (m, k), (n, k2) = a.shape, b.shape
    else:
        (k, m), (k2, n) = a.shape, b.shape
    assert k == k2
    tm, tn, tk = min(tm, m), min(tn, n), min(tk, k)
    assert m % tm == 0 and n % tn == 0 and k % tk == 0
    nk = k // tk
    dot = {"nn": _dot, "nt": _dot_nt, "tn": _dot_tn}[mode]

    def body(a_ref, b_ref, o_ref, acc):
        kk = pl.program_id(2)
        part = dot(a_ref[...].astype(BF16), b_ref[...].astype(BF16))
        if nk == 1:
            o_ref[...] = part.astype(out_dtype)
            return

        @pl.when(kk == 0)
        def _():
            acc[...] = part

        @pl.when(kk > 0)
        def _():
            acc[...] += part

        @pl.when(kk == nk - 1)
        def _():
            o_ref[...] = acc[...].astype(out_dtype)

    if mode == "tn":
        a_spec = pl.BlockSpec((tk, tm), lambda i, j, kk: (kk, i))
    else:
        a_spec = pl.BlockSpec((tm, tk), lambda i, j, kk: (i, kk))
    if mode == "nt":
        b_spec = pl.BlockSpec((tn, tk), lambda i, j, kk: (j, kk))
    else:
        b_spec = pl.BlockSpec((tk, tn), lambda i, j, kk: (kk, j))
    return pl.pallas_call(
        body, name=name,
        out_shape=jax.ShapeDtypeStruct((m, n), out_dtype),
        grid=(m // tm, n // tn, nk),
        in_specs=[a_spec, b_spec],
        out_specs=pl.BlockSpec((tm, tn), lambda i, j, kk: (i, j)),
        scratch_shapes=[pltpu.VMEM((tm, tn) if nk > 1 else (8, LANES), F32)],
        compiler_params=_params(dimension_semantics=("parallel", "parallel", "arbitrary")),
    )(a, b)


def _in_proj(x, g1, w_in_t, b_in):
    tm = 512

    def body(x_ref, g_ref, w_ref, b_ref, qkv_ref, uy_ref, gg_ref, h_ref, h_scr):
        j = pl.program_id(1)

        @pl.when(j == 0)
        def _():
            xv = x_ref[...]
            r = lax.rsqrt(jnp.mean(xv * xv, axis=-1, keepdims=True) + EPS)
            h = ((xv * r) * g_ref[...]).astype(BF16)
            h_scr[...] = h
            h_ref[...] = h

        z = _dot_nt(h_scr[...], w_ref[...]) + b_ref[...]

        @pl.when(j < 3)
        def _():
            qkv_ref[...] = z.astype(BF16)

        @pl.when((j >= 3) & (j < 7))
        def _():
            uy_ref[...] = z

        @pl.when(j >= 7)
        def _():
            gg_ref[...] = z

    return pl.pallas_call(
        body, name="in_proj",
        out_shape=(jax.ShapeDtypeStruct((T, 3 * D_ATT), BF16),
                   jax.ShapeDtypeStruct((T, 2 * D_REC), F32),
                   jax.ShapeDtypeStruct((T, 2 * D), F32),
                   jax.ShapeDtypeStruct((T, D), BF16)),
        grid=(T // tm, N_DZ_TILES),
        in_specs=[pl.BlockSpec((tm, D), lambda i, j: (i, 0)),
                  pl.BlockSpec((1, D), lambda i, j: (0, 0)),
                  pl.BlockSpec((TILE, D), lambda i, j: (j, 0)),
                  pl.BlockSpec((1, TILE), lambda i, j: (0, j))],
        out_specs=(pl.BlockSpec((tm, TILE), lambda i, j: (i, jnp.minimum(j, 2))),
                   pl.BlockSpec((tm, TILE), lambda i, j: (i, jnp.clip(j - 3, 0, 3))),
                   pl.BlockSpec((tm, TILE), lambda i, j: (i, jnp.clip(j - 7, 0, 3))),
                   pl.BlockSpec((tm, D), lambda i, j: (i, 0))),
        scratch_shapes=[pltpu.VMEM((tm, D), BF16)],
        compiler_params=_params(dimension_semantics=("parallel", "arbitrary")),
    )(x, g1, w_in_t, b_in)


def _segment_spec(rows, seg, row_index):
    off, n = seg
    if row_index:
        return pl.BlockSpec((rows, TILE), lambda i, j: (i, jnp.clip(j - off, 0, n - 1)))
    return pl.BlockSpec((rows, TILE), lambda j, kk: (kk, jnp.clip(j - off, 0, n - 1)))


def _dh_norm1_bwd(dz_segments, w_in_t, x, g1, dx1):
    tm = 512

    def body(*refs):
        seg_refs = refs[:7]
        w_ref, x_ref, g_ref, dx1_ref, gx_ref, dg_ref, acc = refs[7:]
        i, kk = pl.program_id(0), pl.program_id(1)

        @pl.when(kk == 0)
        def _():
            acc[...] = jnp.zeros_like(acc)

        for s, (off, n) in enumerate(DZ_SEGMENTS):
            @pl.when((kk >= off) & (kk < off + n))
            def _(s=s):
                acc[...] += _dot(seg_refs[s][...], w_ref[...])

        @pl.when((i == 0) & (kk == 0))
        def _():
            dg_ref[...] = jnp.zeros_like(dg_ref)

        @pl.when(kk == N_DZ_TILES - 1)
        def _():
            xv = x_ref[...]
            r = lax.rsqrt(jnp.mean(xv * xv, axis=-1, keepdims=True) + EPS)
            xh = xv * r
            dh = acc[...]
            dg_ref[...] += jnp.sum(dh * xh, axis=0, keepdims=True)
            gx_ref[...] = dx1_ref[...] + _rms_bwd(dh, xh, r, g_ref[...])

    tok = pl.BlockSpec((tm, D), lambda i, j: (i, 0))
    vec = pl.BlockSpec((1, D), lambda i, j: (0, 0))
    return pl.pallas_call(
        body, name="dh_norm1_bwd",
        out_shape=(jax.ShapeDtypeStruct((T, D), F32), jax.ShapeDtypeStruct((1, D), F32)),
        grid=(T // tm, N_DZ_TILES),
        in_specs=[_segment_spec(tm, seg, True) for seg in DZ_SEGMENTS]
        + [pl.BlockSpec((TILE, D), lambda i, j: (j, 0)), tok, vec, tok],
        out_specs=(tok, vec),
        scratch_shapes=[pltpu.VMEM((tm, D), F32)],
        compiler_params=_params(dimension_semantics=("arbitrary", "arbitrary")),
    )(*dz_segments, w_in_t, x, g1, dx1)


def _grad_w_in(dz_segments, h):
    tk = 1024
    nk = T // tk

    def body(*refs):
        seg_refs = refs[:7]
        h_ref, gw_ref, gb_ref, acc = refs[7:]
        j, kk = pl.program_id(0), pl.program_id(1)

        @pl.when(kk == 0)
        def _():
            acc[...] = jnp.zeros_like(acc)
            gb_ref[...] = jnp.zeros_like(gb_ref)

        for s, (off, n) in enumerate(DZ_SEGMENTS):
            @pl.when((j >= off) & (j < off + n))
            def _(s=s):
                a = seg_refs[s][...]
                acc[...] += _dot_tn(a, h_ref[...])
                gb_ref[...] += jnp.sum(a.astype(F32), axis=0, keepdims=True)

        @pl.when(kk == nk - 1)
        def _():
            gw_ref[...] = acc[...].astype(BF16)

    return pl.pallas_call(
        body, name="grad_w_in",
        out_shape=(jax.ShapeDtypeStruct((D_IN, D), BF16), jax.ShapeDtypeStruct((1, D_IN), F32)),
        grid=(N_DZ_TILES, nk),
        in_specs=[_segment_spec(tk, seg, False) for seg in DZ_SEGMENTS]
        + [pl.BlockSpec((tk, D), lambda j, kk: (kk, 0))],
        out_specs=(pl.BlockSpec((TILE, D), lambda j, kk: (j, 0)),
                   pl.BlockSpec((1, TILE), lambda j, kk: (0, j))),
        scratch_shapes=[pltpu.VMEM((TILE, D), F32)],
        compiler_params=_params(dimension_semantics=("parallel", "arbitrary")),
    )(*dz_segments, h)


def _rpb_expand(rpb):
    rpb2 = jnp.pad(rpb.reshape(N_HEADS * N_RPB_R, N_RPB_C), ((0, 0), (0, 128 - N_RPB_C)))
    table = jnp.asarray(_toeplitz_table())

    def body(r_ref, e_ref, o_ref):
        o_ref[...] = _dot_exact(r_ref[...], e_ref[...])

    tb = pl.pallas_call(
        body, name="rpb_expand",
        out_shape=jax.ShapeDtypeStruct((N_HEADS * N_RPB_R, GRID_W * GRID_W), F32),
    )(rpb2, table)
    tb = tb.reshape(N_HEADS, N_RPB_R, GRID_W, GRID_W)
    variants = []
    for oi in range(WIN_H):
        sl = tb[:, WIN_H - 1 - oi: 2 * WIN_H - 1 - oi]
        sl = sl.transpose(0, 2, 1, 3).reshape(N_HEADS // 2, 2 * GRID_W, KWIN)
        variants.append(sl)
    return jnp.stack(variants, axis=0)


def _rpb_reduce(gbias):
    g = gbias.reshape(N_HEADS // 2, WIN_H, 2, GRID_W, WIN_H, GRID_W)
    g = g.transpose(0, 2, 1, 4, 3, 5).reshape(N_HEADS, WIN_H * WIN_H, GRID_W * GRID_W)
    table = jnp.asarray(_toeplitz_table().T.copy())
    shift = jnp.asarray(_row_shift_table())

    def body(g_ref, e_ref, s_ref, o_ref):
        r = _dot_exact(g_ref[0], e_ref[...])
        o_ref[0] = _dot_exact(s_ref[...], r)

    out = pl.pallas_call(
        body, name="rpb_reduce",
        out_shape=jax.ShapeDtypeStruct((N_HEADS, 16, 128), F32),
        grid=(N_HEADS,),
        in_specs=[pl.BlockSpec((1, WIN_H * WIN_H, GRID_W * GRID_W), lambda h: (h, 0, 0)),
                  pl.BlockSpec((GRID_W * GRID_W, 128), lambda h: (0, 0)),
                  pl.BlockSpec((16, WIN_H * WIN_H), lambda h: (0, 0))],
        out_specs=pl.BlockSpec((1, 16, 128), lambda h: (h, 0, 0)),
        compiler_params=_params(dimension_semantics=("arbitrary",)),
    )(g, table, shift)
    return out[:, :N_RPB_R, :N_RPB_C]


def _att_scores(q_ref, k_ref, bias_ref, valid, hmask, r):
    rs = jnp.clip(r - WIN_H // 2, 0, ROWS - WIN_H)
    oi = r - rs
    q0 = pl.multiple_of(r * GRID_W, GRID_W)
    k0 = pl.multiple_of(rs * GRID_W, GRID_W)
    q_r = q_ref[pl.ds(q0, GRID_W), :]
    q2 = jnp.where(hmask, jnp.concatenate([q_r, q_r], axis=0), jnp.zeros((), BF16))
    kw = k_ref[pl.ds(k0, KWIN), :]
    s = _dot_nt(q2, kw) * (DH ** -0.5) + bias_ref[oi, 0]
    s = jnp.where(valid, s, -1e30)
    m = jnp.max(s, axis=-1, keepdims=True)
    p = jnp.exp(s - m)
    p = p / jnp.sum(p, axis=-1, keepdims=True)
    return p, q2, kw, q0, k0, oi


def _att_fwd(qkv, bias):
    valid_np, hmask_np = _att_tables()

    def body(q_ref, k_ref, v_ref, bias_ref, valid_ref, hmask_ref, o_ref):
        valid = valid_ref[...] > 0.5
        hmask = hmask_ref[...] > 0.5
        first_head = lax.broadcasted_iota(jnp.int32, (GRID_W, 2 * DH), 1) < DH

        def row(r, carry):
            p, _, _, q0, k0, _ = _att_scores(q_ref, k_ref, bias_ref, valid, hmask, r)
            o2 = _dot(p.astype(BF16), v_ref[pl.ds(k0, KWIN), :])
            o_ref[pl.ds(q0, GRID_W), :] = jnp.where(first_head, o2[:GRID_W], o2[GRID_W:]).astype(BF16)
            return carry

        lax.fori_loop(0, ROWS, row, 0, unroll=4)

    col = lambda off: pl.BlockSpec((T, 2 * DH), lambda hp: (0, hp + off))
    return pl.pallas_call(
        body, name="att_fwd",
        out_shape=jax.ShapeDtypeStruct((T, D_ATT), BF16),
        grid=(N_HEADS // 2,),
        in_specs=[col(0), col(4), col(8),
                  pl.BlockSpec((WIN_H, 1, 2 * GRID_W, KWIN), lambda hp: (0, hp, 0, 0)),
                  pl.BlockSpec((2 * GRID_W, KWIN), lambda hp: (0, 0)),
                  pl.BlockSpec((2 * DH, 2 * DH), lambda hp: (0, 0))],
        out_specs=pl.BlockSpec((T, 2 * DH), lambda hp: (0, hp)),
        compiler_params=_params(dimension_semantics=("parallel",)),
    )(qkv, qkv, qkv, bias, jnp.asarray(valid_np), jnp.asarray(hmask_np))


def _att_bwd(qkv, bias, datt):
    valid_np, hmask_np = _att_tables()

    def body(q_ref, k_ref, v_ref, do_ref, bias_ref, valid_ref, hmask_ref,
             dq_ref, dk_ref, dv_ref, gb_ref, dk_acc, dv_acc):
        valid = valid_ref[...] > 0.5
        hmask = hmask_ref[...] > 0.5
        first_head = lax.broadcasted_iota(jnp.int32, (GRID_W, 2 * DH), 1) < DH
        dk_acc[...] = jnp.zeros_like(dk_acc)
        dv_acc[...] = jnp.zeros_like(dv_acc)
        gb_ref[...] = jnp.zeros_like(gb_ref)

        def row(r, carry):
            p, q2, kw, q0, k0, oi = _att_scores(q_ref, k_ref, bias_ref, valid, hmask, r)
            do_r = do_ref[pl.ds(q0, GRID_W), :]
            do2 = jnp.where(hmask, jnp.concatenate([do_r, do_r], axis=0), jnp.zeros((), BF16))
            vw = v_ref[pl.ds(k0, KWIN), :]
            dp = _dot_nt(do2, vw)
            ds = p * (dp - jnp.sum(dp * p, axis=-1, keepdims=True))
            p16 = p.astype(BF16)
            ds16 = ds.astype(BF16)
            dv_acc[pl.ds(k0, KWIN), :] += _dot_tn(p16, do2)
            dk_acc[pl.ds(k0, KWIN), :] += _dot_tn(ds16, q2) * (DH ** -0.5)
            dq2 = _dot(ds16, kw) * (DH ** -0.5)
            dq_ref[pl.ds(q0, GRID_W), :] = jnp.where(first_head, dq2[:GRID_W], dq2[GRID_W:]).astype(BF16)
            gb_ref[0, oi] += ds
            return carry

        lax.fori_loop(0, ROWS, row, 0, unroll=4)
        dk_ref[...] = dk_acc[...].astype(BF16)
        dv_ref[...] = dv_acc[...].astype(BF16)

    col = lambda off: pl.BlockSpec((T, 2 * DH), lambda hp: (0, hp + off))
    out_col = pl.BlockSpec((T, 2 * DH), lambda hp: (0, hp))
    return pl.pallas_call(
        body, name="att_bwd",
        out_shape=(jax.ShapeDtypeStruct((T, D_ATT), BF16),) * 3
        + (jax.ShapeDtypeStruct((N_HEADS // 2, WIN_H, 2 * GRID_W, KWIN), F32),),
        grid=(N_HEADS // 2,),
        in_specs=[col(0), col(4), col(8), col(0),
                  pl.BlockSpec((WIN_H, 1, 2 * GRID_W, KWIN), lambda hp: (0, hp, 0, 0)),
                  pl.BlockSpec((2 * GRID_W, KWIN), lambda hp: (0, 0)),
                  pl.BlockSpec((2 * DH, 2 * DH), lambda hp: (0, 0))],
        out_specs=(out_col, out_col, out_col,
                   pl.BlockSpec((1, WIN_H, 2 * GRID_W, KWIN), lambda hp: (hp, 0, 0, 0))),
        scratch_shapes=[pltpu.VMEM((T, 2 * DH), F32), pltpu.VMEM((T, 2 * DH), F32)],
        compiler_params=_params(dimension_semantics=("parallel",)),
    )(qkv, qkv, qkv, datt, bias, jnp.asarray(valid_np), jnp.asarray(hmask_np))


def _conv_taps(up):
    return (_shift_rows(up, 2), _shift_rows(up, 1), up, _shift_rows(up, -1))


def _pair_block_diag(w_pair, dup, same_half):
    return jnp.where(same_half, _dot(w_pair.astype(BF16), dup), 0.0).astype(BF16)


def _gates(u, u16, wa, ba, wi, bi, lam):
    r = _sigmoid(_dot(u16, wa) + ba)
    ig = _sigmoid(_dot(u16, wi) + bi)
    sp = _softplus(-lam)
    log_a = (-LRU_C) * r * sp
    a = jnp.exp(log_a)
    mult = jnp.sqrt(jnp.maximum(_one_minus_square(log_a, a), 0.0))
    return r, ig, sp, a, mult


SCAN_BLOCKS = 2


def _scans(jobs):
    c = jobs[0][0].shape[1]
    nblk = T // 8
    rows = lax.broadcasted_iota(jnp.int32, (8, c), 0)

    def block(a, b, reverse):
        for s in (1, 2, 4):
            if reverse:
                keep = rows < 8 - s
                a_s = jnp.where(keep, pltpu.roll(a, 8 - s, 0), 1.0)
                b_s = jnp.where(keep, pltpu.roll(b, 8 - s, 0), 0.0)
            else:
                keep = rows >= s
                a_s = jnp.where(keep, pltpu.roll(a, s, 0), 1.0)
                b_s = jnp.where(keep, pltpu.roll(b, s, 0), 0.0)
            b = a * b_s + b
            a = a * a_s
        return a, b

    def step(i, carry):
        out = []
        for (a_ref, b_ref, h_ref, reverse), h_prev in zip(jobs, carry):
            for u in range(SCAN_BLOCKS):
                blk = i * SCAN_BLOCKS + u
                if reverse:
                    blk = nblk - 1 - blk
                t0 = pl.multiple_of(blk * 8, 8)
                a, b = block(a_ref[pl.ds(t0, 8), :], b_ref[pl.ds(t0, 8), :], reverse)
                h = a * h_prev + b
                h_ref[pl.ds(t0, 8), :] = h
                h_prev = jnp.broadcast_to(h[0:1] if reverse else h[7:8], (8, c))
            out.append(h_prev)
        return tuple(out)

    lax.fori_loop(0, nblk // SCAN_BLOCKS, step, tuple(jnp.zeros((8, c), F32) for _ in jobs))


def _rec_specs():
    tok = lambda off: pl.BlockSpec((T, CG), lambda g: (0, g + off))
    per_ch = lambda rows: pl.BlockSpec((rows, CG), lambda g: (0, g))
    wspec = pl.BlockSpec((2, 1, CG, REC_BLOCK), lambda g: (0, g, 0, 0))
    const = lambda shape: pl.BlockSpec(shape, lambda g: (0, 0))
    return tok, per_ch, wspec, const


def _rec_fwd(uy, conv_w, conv_b, w_a, b_a, w_i, b_i, lam):
    tok, per_ch, wspec, const = _rec_specs()

    def body(up_ref, yb_ref, cw_ref, cb_ref, wa_ref, ba_ref, wi_ref, bi_ref, lam_ref, dup_ref, half_ref,
             hf_ref, hb_ref, yrec_ref, a_f, bx_f, a_b, bx_b):
        dup = dup_ref[...]
        same_half = half_ref[...] > 0.5
        taps = _conv_taps(up_ref[...])
        u = cb_ref[...]
        for j in range(4):
            u = u + taps[j] * cw_ref[j:j + 1, :]
        u16 = u.astype(BF16)
        for d, (a_s, bx_s) in enumerate(((a_f, bx_f), (a_b, bx_b))):
            wa = _pair_block_diag(wa_ref[d, 0], dup, same_half)
            wi = _pair_block_diag(wi_ref[d, 0], dup, same_half)
            _, ig, _, a, mult = _gates(u, u16, wa, ba_ref[d:d + 1, :], wi, bi_ref[d:d + 1, :],
                                       lam_ref[d:d + 1, :])
            a_s[...] = a
            bx_s[...] = mult * (ig * u)
        _scans([(a_f, bx_f, hf_ref, False), (a_b, bx_b, hb_ref, True)])
        gelu, _ = _gelu_and_grad(yb_ref[...])
        yrec_ref[...] = ((hf_ref[...] + hb_ref[...]) * gelu).astype(BF16)

    return pl.pallas_call(
        body, name="rec_fwd",
        out_shape=(jax.ShapeDtypeStruct((T, D_REC), F32), jax.ShapeDtypeStruct((T, D_REC), F32),
                   jax.ShapeDtypeStruct((T, D_REC), BF16)),
        grid=(N_CG,),
        in_specs=[tok(0), tok(N_CG), per_ch(4), per_ch(1), wspec, per_ch(2), wspec, per_ch(2), per_ch(2),
                  const((REC_BLOCK, CG)), const((CG, CG))],
        out_specs=(tok(0), tok(0), tok(0)),
        scratch_shapes=[pltpu.VMEM((T, CG), F32)] * 4,
        compiler_params=_params(dimension_semantics=("parallel",)),
    )(uy, uy, conv_w, conv_b, w_a, b_a, w_i, b_i, lam,
      jnp.asarray(_dup_table(), BF16), jnp.asarray(_pair_mask()))


def _rec_bwd(uy, hf, hb, dyrec, conv_w, conv_b, w_a, b_a, w_i, b_i, lam):
    tok, per_ch, wspec, const = _rec_specs()

    def body(up_ref, yb_ref, hf_ref, hb_ref, dy_ref, cw_ref, cb_ref, wa_ref, ba_ref, wi_ref, bi_ref,
             lam_ref, dup_ref, dupt_ref, half_ref,
             dup_out, dyb_ref, dcw_ref, dcb_ref, dwa_ref, dba_ref, dwi_ref, dbi_ref, dlam_ref,
             a_s0, a_s1, dh_s, g_s0, g_s1):
        dup = dup_ref[...]
        dup_t = dupt_ref[...]
        same_half = half_ref[...] > 0.5
        taps = _conv_taps(up_ref[...])
        u = cb_ref[...]
        for j in range(4):
            u = u + taps[j] * cw_ref[j:j + 1, :]
        u16 = u.astype(BF16)
        gelu, dgelu = _gelu_and_grad(yb_ref[...])
        dy = dy_ref[...]
        dyb_ref[...] = (dy * (hf_ref[...] + hb_ref[...]) * dgelu).astype(BF16)
        dh_s[...] = dy * gelu
        gate_values = []
        for d, a_s in enumerate((a_s0, a_s1)):
            wa = _pair_block_diag(wa_ref[d, 0], dup, same_half)
            wi = _pair_block_diag(wi_ref[d, 0], dup, same_half)
            lam_d = lam_ref[d:d + 1, :]
            r, ig, sp, a, mult = _gates(u, u16, wa, ba_ref[d:d + 1, :], wi, bi_ref[d:d + 1, :], lam_d)
            a_s[...] = _shift_rows(a, 1 if d == 1 else -1)
            gate_values.append((wa, wi, lam_d, r, ig, sp, a, mult))
        _scans([(a_s0, dh_s, g_s0, True), (a_s1, dh_s, g_s1, False)])
        du = jnp.zeros((T, CG), F32)
        for d, g_s in enumerate((g_s0, g_s1)):
            reverse = d == 1
            wa, wi, lam_d, r, ig, sp, a, mult = gate_values[d]
            g = g_s[...]
            h_prev = _shift_rows(hb_ref[...], -1) if reverse else _shift_rows(hf_ref[...], 1)
            da = g * h_prev
            dmult = g * (ig * u)
            dig = g * mult * u
            du = du + g * mult * ig
            dmult_dlog = jnp.where(mult > 0.0, -(a * a) / mult, 0.0)
            dlog_a = da * a + dmult * dmult_dlog
            dr = dlog_a * ((-LRU_C) * sp)
            dsp = jnp.sum(dlog_a * ((-LRU_C) * r), axis=0, keepdims=True)
            dlam_ref[d:d + 1, :] = dsp * (-_sigmoid(-lam_d))
            dga = dr * r * (1.0 - r)
            dgi = dig * ig * (1.0 - ig)
            dga16 = dga.astype(BF16)
            dgi16 = dgi.astype(BF16)
            du = du + _dot_nt(dga16, wa) + _dot_nt(dgi16, wi)
            dwa_ref[d, 0] = _dot_exact(jnp.where(same_half, _dot_tn(u16, dga16), 0.0), dup_t)
            dwi_ref[d, 0] = _dot_exact(jnp.where(same_half, _dot_tn(u16, dgi16), 0.0), dup_t)
            dba_ref[d:d + 1, :] = jnp.sum(dga, axis=0, keepdims=True)
            dbi_ref[d:d + 1, :] = jnp.sum(dgi, axis=0, keepdims=True)
        dcb_ref[...] = jnp.sum(du, axis=0, keepdims=True)
        for j in range(4):
            dcw_ref[j:j + 1, :] = jnp.sum(du * taps[j], axis=0, keepdims=True)
        dup_in = (_shift_rows(du, -2) * cw_ref[0:1, :] + _shift_rows(du, -1) * cw_ref[1:2, :]
                  + du * cw_ref[2:3, :] + _shift_rows(du, 1) * cw_ref[3:4, :])
        dup_out[...] = dup_in.astype(BF16)

    wshape = jax.ShapeDtypeStruct((2, N_CG, CG, REC_BLOCK), F32)
    vec = lambda rows: jax.ShapeDtypeStruct((rows, D_REC), F32)
    dup_np = _dup_table()
    return pl.pallas_call(
        body, name="rec_bwd",
        out_shape=(jax.ShapeDtypeStruct((T, D_REC), BF16), jax.ShapeDtypeStruct((T, D_REC), BF16),
                   vec(4), vec(1), wshape, vec(2), wshape, vec(2), vec(2)),
        grid=(N_CG,),
        in_specs=[tok(0), tok(N_CG), tok(0), tok(0), tok(0),
                  per_ch(4), per_ch(1), wspec, per_ch(2), wspec, per_ch(2), per_ch(2),
                  const((REC_BLOCK, CG)), const((CG, REC_BLOCK)), const((CG, CG))],
        out_specs=(tok(0), tok(0), per_ch(4), per_ch(1), wspec, per_ch(2), wspec, per_ch(2), per_ch(2)),
        scratch_shapes=[pltpu.VMEM((T, CG), F32)] * 5,
        compiler_params=_params(dimension_semantics=("parallel",)),
    )(uy, uy, hf, hb, dyrec, conv_w, conv_b, w_a, b_a, w_i, b_i, lam,
      jnp.asarray(dup_np, BF16), jnp.asarray(dup_np.T.copy()), jnp.asarray(_pair_mask()))


TM_MIX = 256


def _mix_specs():
    tok = lambda width, blk=0: pl.BlockSpec((TM_MIX, width), lambda i: (i, blk))
    full = lambda shape: pl.BlockSpec(shape, lambda i: (0, 0))
    return tok, full


def _mix_fwd(x, att, yrec, gg, w_att_o_t, w_rec_o, w_out):
    tok, full = _mix_specs()

    def body(x_ref, att_ref, yr_ref, ga_ref, gr_ref, wao_ref, wro_ref, wo_ref, x1_ref, mixed_ref):
        y_att = _dot_nt(att_ref[...], wao_ref[...])
        y_rec = _dot(yr_ref[...], wro_ref[...])
        mixed = (_sigmoid(ga_ref[...]) * y_att + _sigmoid(gr_ref[...]) * y_rec).astype(BF16)
        mixed_ref[...] = mixed
        x1_ref[...] = x_ref[...] + _dot(mixed, wo_ref[...])

    return pl.pallas_call(
        body, name="mix_fwd",
        out_shape=(jax.ShapeDtypeStruct((T, D), F32), jax.ShapeDtypeStruct((T, D), BF16)),
        grid=(T // TM_MIX,),
        in_specs=[tok(D), tok(D_ATT), tok(D_REC), tok(D, 0), tok(D, 1),
                  full((D, D_ATT)), full((D_REC, D)), full((D, D))],
        out_specs=(tok(D), tok(D)),
        compiler_params=_params(dimension_semantics=("parallel",)),
    )(x, att, yrec, gg, gg, w_att_o_t, w_rec_o, w_out)


def _mix_bwd(dx1, att, yrec, gg, w_att_o_t, w_rec_o, w_out):
    tok, full = _mix_specs()

    def body(dx_ref, att_ref, yr_ref, ga_ref, gr_ref, wao_ref, wro_ref, wo_ref,
             dga_ref, dgr_ref, dya_ref, dyr_ref, datt_ref, dyrp_ref):
        dmixed = _dot_nt(dx_ref[...].astype(BF16), wo_ref[...])
        y_att = _dot_nt(att_ref[...], wao_ref[...])
        y_rec = _dot(yr_ref[...], wro_ref[...])
        sa = _sigmoid(ga_ref[...])
        sr = _sigmoid(gr_ref[...])
        dga_ref[...] = (dmixed * y_att * sa * (1.0 - sa)).astype(BF16)
        dgr_ref[...] = (dmixed * y_rec * sr * (1.0 - sr)).astype(BF16)
        dya = (dmixed * sa).astype(BF16)
        dyr = (dmixed * sr).astype(BF16)
        dya_ref[...] = dya
        dyr_ref[...] = dyr
        datt_ref[...] = _dot(dya, wao_ref[...]).astype(BF16)
        dyrp_ref[...] = _dot_nt(dyr, wro_ref[...])

    return pl.pallas_call(
        body, name="mix_bwd",
        out_shape=(jax.ShapeDtypeStruct((T, D), BF16), jax.ShapeDtypeStruct((T, D), BF16),
                   jax.ShapeDtypeStruct((T, D), BF16), jax.ShapeDtypeStruct((T, D), BF16),
                   jax.ShapeDtypeStruct((T, D_ATT), BF16), jax.ShapeDtypeStruct((T, D_REC), F32)),
        grid=(T // TM_MIX,),
        in_specs=[tok(D), tok(D_ATT), tok(D_REC), tok(D, 0), tok(D, 1),
                  full((D, D_ATT)), full((D_REC, D)), full((D, D))],
        out_specs=(tok(D), tok(D), tok(D), tok(D), tok(D_ATT), tok(D_REC)),
        compiler_params=_params(dimension_semantics=("parallel",)),
    )(dx1, att, yrec, gg, gg, w_att_o_t, w_rec_o, w_out)


TM_FFN = 256
FF_CHUNK = 1024


def _ffn_loss(x1, target, g2, gf, w_ff1_t, w_ff2):
    n_chunks = D_FF // FF_CHUNK

    def body(x1_ref, tg_ref, g2_ref, gf_ref, w1_hbm, w2_hbm,
             loss_ref, dx1_ref, h2_ref, act_ref, dpre_ref, dx2_ref, dg2_ref, dgf_ref,
             w1, w2, relu_s):
        i = pl.program_id(0)

        @pl.when(i == 0)
        def _():
            pltpu.sync_copy(w1_hbm, w1)
            pltpu.sync_copy(w2_hbm, w2)
            loss_ref[...] = jnp.zeros_like(loss_ref)
            dg2_ref[...] = jnp.zeros_like(dg2_ref)
            dgf_ref[...] = jnp.zeros_like(dgf_ref)

        x1v = x1_ref[...]
        r2 = lax.rsqrt(jnp.mean(x1v * x1v, axis=-1, keepdims=True) + EPS)
        xh2 = x1v * r2
        h2 = (xh2 * g2_ref[...]).astype(BF16)
        h2_ref[...] = h2
        x2 = x1v
        for c in range(n_chunks):
            ff = slice(c * FF_CHUNK, (c + 1) * FF_CHUNK)
            rl = jnp.maximum(_dot_nt(h2, w1[ff, :]), 0.0)
            relu_s[:, ff] = rl
            act = (rl * rl).astype(BF16)
            act_ref[:, ff] = act
            x2 = x2 + _dot(act, w2[ff, :])
        r3 = lax.rsqrt(jnp.mean(x2 * x2, axis=-1, keepdims=True) + EPS)
        xh3 = x2 * r3
        err = xh3 * gf_ref[...] - tg_ref[...]
        loss_ref[...] += 0.5 * jnp.sum(jnp.mean(err * err, axis=-1, keepdims=True))
        dy = err * (1.0 / D)
        dgf_ref[...] += jnp.sum(dy * xh3, axis=0, keepdims=True)
        dx2 = _rms_bwd(dy, xh3, r3, gf_ref[...])
        dx2_16 = dx2.astype(BF16)
        dx2_ref[...] = dx2_16
        dh2 = jnp.zeros((TM_FFN, D), F32)
        for c in range(n_chunks):
            ff = slice(c * FF_CHUNK, (c + 1) * FF_CHUNK)
            dpre = (_dot_nt(dx2_16, w2[ff, :]) * (2.0 * relu_s[:, ff])).astype(BF16)
            dpre_ref[:, ff] = dpre
            dh2 = dh2 + _dot(dpre, w1[ff, :])
        dg2_ref[...] += jnp.sum(dh2 * xh2, axis=0, keepdims=True)
        dx1_ref[...] = dx2 + _rms_bwd(dh2, xh2, r2, g2_ref[...])

    tok = lambda width: pl.BlockSpec((TM_FFN, width), lambda i: (i, 0))
    vec = pl.BlockSpec((1, D), lambda i: (0, 0))
    hbm = pl.BlockSpec(memory_space=pl.ANY)
    return pl.pallas_call(
        body, name="ffn_loss",
        out_shape=(jax.ShapeDtypeStruct((8, 128), F32), jax.ShapeDtypeStruct((T, D), F32),
                   jax.ShapeDtypeStruct((T, D), BF16), jax.ShapeDtypeStruct((T, D_FF), BF16),
                   jax.ShapeDtypeStruct((T, D_FF), BF16), jax.ShapeDtypeStruct((T, D), BF16),
                   jax.ShapeDtypeStruct((1, D), F32), jax.ShapeDtypeStruct((1, D), F32)),
        grid=(T // TM_FFN,),
        in_specs=[tok(D), tok(D), vec, vec, hbm, hbm],
        out_specs=(pl.BlockSpec((8, 128), lambda i: (0, 0)), tok(D), tok(D), tok(D_FF), tok(D_FF), tok(D),
                   vec, vec),
        scratch_shapes=[pltpu.VMEM((D_FF, D), BF16), pltpu.VMEM((D_FF, D), BF16),
                        pltpu.VMEM((TM_FFN, D_FF), F32)],
        compiler_params=_params(dimension_semantics=("arbitrary",)),
    )(x1, target, g2, gf, w_ff1_t, w_ff2)


def _local_step(x, target, p):
    bias = _rpb_expand(p["rpb"])
    pairs = lambda w: w.reshape(2, N_CG, CG, REC_BLOCK)
    w_a, w_i = pairs(p["w_rg_a"]), pairs(p["w_rg_i"])
    rec_params = (p["conv_w"], p["conv_b"], w_a, p["b_rg_a"], w_i, p["b_rg_i"], p["lru_lambda"])

    qkv, uy, gg, h = _in_proj(x, p["ln1_g"], p["w_in_t"], p["b_in"])
    att = _att_fwd(qkv, bias)
    hf, hb, yrec = _rec_fwd(uy, *rec_params)
    x1, mixed = _mix_fwd(x, att, yrec, gg, p["w_att_o_t"], p["w_rec_o"], p["w_out"])
    loss8, dx1, h2, act, dpre, dx2, g_ln2, g_lnf = _ffn_loss(
        x1, target, p["ln2_g"], p["lnf_g"], p["w_ff1_t"], p["w_ff2"])

    dga, dgr, dya, dyr, datt, dyrp = _mix_bwd(dx1, att, yrec, gg, p["w_att_o_t"], p["w_rec_o"], p["w_out"])
    dup, dyb, g_cw, g_cb, g_wa, g_ba, g_wi, g_bi, g_lam = _rec_bwd(uy, hf, hb, dyrp, *rec_params)
    dq, dk, dv, gbias = _att_bwd(qkv, bias, datt)
    dz = (dq, dk, dv, dup, dyb, dga, dgr)
    grad_x, g_ln1 = _dh_norm1_bwd(dz, p["w_in_t"], x, p["ln1_g"], dx1)
    g_w_in_t, g_b_in = _grad_w_in(dz, h)

    blocks = lambda g: g.reshape(2, N_REC_BLOCKS, REC_BLOCK, REC_BLOCK)
    grads = {
        "ln1_g": g_ln1, "w_in_t": g_w_in_t, "b_in": g_b_in, "rpb": _rpb_reduce(gbias),
        "w_att_o_t": _matmul(dya, att, "tn", BF16, "g_w_att_o"),
        "conv_w": g_cw, "conv_b": g_cb, "w_rg_a": blocks(g_wa), "b_rg_a": g_ba,
        "w_rg_i": blocks(g_wi), "b_rg_i": g_bi, "lru_lambda": g_lam,
        "w_rec_o": _matmul(yrec, dyr, "tn", BF16, "g_w_rec_o"),
        "w_out": _matmul(mixed, dx1, "tn", BF16, "g_w_out"),
        "ln2_g": g_ln2,
        "w_ff1_t": _matmul(dpre, h2, "tn", BF16, "g_w_ff1"),
        "w_ff2": _matmul(act, dx2, "tn", BF16, "g_w_ff2"),
        "lnf_g": g_lnf,
    }
    return loss8[0:1, 0:1], grad_x, grads


MESH_ID = pl.DeviceIdType.MESH
ANY = pl.BlockSpec(memory_space=pl.ANY)

CHAN_BLOCK_ROWS = 32
SECTIONS = (("w_in_t", 704, D), ("w_rec_o", 128, D), ("w_out", 128, D), ("w_ff1_t", 512, D),
            ("w_ff2", 512, D), ("chan", CHAN_BLOCK_ROWS, D), ("w_att_o_t", 128, D_ATT))
N_SEC = len(SECTIONS)
N_CHAN_ROWS = 10
CHAN = (("conv_w", 4), ("b_rg_a", 2), ("b_rg_i", 2), ("lru_lambda", 2))


def _position():
    return lax.axis_index("x"), lax.axis_index("y"), lax.axis_index("c")


def _other_chips(x, y):
    return [(1 - x, y), (x, 1 - y), (1 - x, 1 - y)]


def _block_of(ref, dev, rows):
    return ref.at[pl.ds(pl.multiple_of(dev * rows, 16), rows)]


def _all_gather(shards, name):
    ns = len(shards)

    def body(*refs):
        x_refs, out_refs = refs[:ns], refs[ns:2 * ns]
        send_sems, recv_sems, local_sems = refs[2 * ns:]
        x, y, c = _position()
        me, sibling = (x, y, c), (x, y, 1 - c)
        chips = _other_chips(x, y)

        def rows(s, px, py, pc):
            return _block_of(out_refs[s], 4 * px + 2 * py + pc, shards[s].shape[0])

        def copy(k, s, block, to, from_shard=False):
            return pltpu.make_async_remote_copy(
                src_ref=x_refs[s] if from_shard else rows(s, *block), dst_ref=rows(s, *block),
                send_sem=send_sems.at[k * ns + s], recv_sem=recv_sems.at[k * ns + s],
                device_id=to, device_id_type=MESH_ID)

        sections = range(ns)
        mine = [pltpu.make_async_copy(x_refs[s], rows(s, *me), local_sems.at[s]) for s in sections]
        first = [copy(0, s, me, sibling, True) for s in sections]
        first += [copy(1 + j, s, me, (*chip, c), True) for j, chip in enumerate(chips) for s in sections]
        for cp in mine + first:
            cp.start()
        passed = []
        for j, chip in enumerate(chips):
            for s in sections:
                copy(1 + j, s, (*chip, c), me).wait_recv()
                passed.append(copy(4 + j, s, (*chip, c), sibling))
                passed[-1].start()
        for s in sections:
            copy(0, s, sibling, me).wait_recv()
        for j, chip in enumerate(chips):
            for s in sections:
                copy(4 + j, s, (*chip, 1 - c), me).wait_recv()
        for cp in first + passed:
            cp.wait_send()
        for cp in mine:
            cp.wait()

    return pl.pallas_call(
        body, name=name,
        out_shape=tuple(jax.ShapeDtypeStruct((N_DEV * s.shape[0], s.shape[1]), s.dtype) for s in shards),
        in_specs=[ANY] * ns, out_specs=(ANY,) * ns,
        scratch_shapes=[pltpu.SemaphoreType.DMA((7 * ns,)), pltpu.SemaphoreType.DMA((7 * ns,)),
                        pltpu.SemaphoreType.DMA((ns,))],
    )(*shards)


def _pair_exchange(grads):
    def body(*refs):
        g_refs, land = refs[:N_SEC], refs[N_SEC:2 * N_SEC]
        send_sems, recv_sems = refs[2 * N_SEC:]
        x, y, c = _position()
        copies = [pltpu.make_async_remote_copy(
            src_ref=_block_of(g_refs[s], 2 * k + 1 - c, rows), dst_ref=land[s].at[k],
            send_sem=send_sems.at[k * N_SEC + s], recv_sem=recv_sems.at[k * N_SEC + s],
            device_id=(x, y, 1 - c), device_id_type=MESH_ID)
            for k in range(N_CHIPS) for s, (_, rows, _) in enumerate(SECTIONS)]
        for cp in copies:
            cp.start()
        for cp in copies:
            cp.wait_recv()
        for cp in copies:
            cp.wait_send()

    n = N_CHIPS * N_SEC
    return pl.pallas_call(
        body, name="grad_pair_exchange",
        out_shape=tuple(jax.ShapeDtypeStruct((N_CHIPS, rows, cols), BF16) for _, rows, cols in SECTIONS),
        in_specs=[ANY] * N_SEC, out_specs=(ANY,) * N_SEC,
        scratch_shapes=[pltpu.SemaphoreType.DMA((n,)), pltpu.SemaphoreType.DMA((n,))],
    )(*grads)


def _pair_add(grads, got, core):
    def body(core_ref, *refs):
        g_refs, got_refs, p_refs = refs[:N_SEC], refs[N_SEC:2 * N_SEC], refs[2 * N_SEC:]
        for s in range(N_SEC):
            p_refs[s][0] = (g_refs[s][...].astype(F32) + got_refs[s][0].astype(F32)).astype(BF16)

    slot = [pl.BlockSpec((1, rows, cols), lambda k, c: (k, 0, 0)) for _, rows, cols in SECTIONS]
    return pl.pallas_call(
        body, name="grad_pair_add",
        out_shape=tuple(jax.ShapeDtypeStruct((N_CHIPS, rows, cols), BF16) for _, rows, cols in SECTIONS),
        grid_spec=pltpu.PrefetchScalarGridSpec(
            num_scalar_prefetch=1, grid=(N_CHIPS,),
            in_specs=[pl.BlockSpec((rows, cols), lambda k, c: (2 * k + c[0], 0)) for _, rows, cols in SECTIONS]
            + slot,
            out_specs=tuple(slot)),
        compiler_params=_params(dimension_semantics=("parallel",)),
    )(core, *grads, *got)


def _chip_exchange(parts):
    def body(*refs):
        p_refs, land = refs[:N_SEC], refs[N_SEC:2 * N_SEC]
        send_sems, recv_sems = refs[2 * N_SEC:]
        x, y, c = _position()
        copies = [pltpu.make_async_remote_copy(
            src_ref=p_refs[s].at[2 * cx + cy], dst_ref=land[s].at[j],
            send_sem=send_sems.at[j * N_SEC + s], recv_sem=recv_sems.at[j * N_SEC + s],
            device_id=(cx, cy, c), device_id_type=MESH_ID)
            for j, (cx, cy) in enumerate(_other_chips(x, y)) for s in range(N_SEC)]
        for cp in copies:
            cp.start()
        for cp in copies:
            cp.wait_recv()
        for cp in copies:
            cp.wait_send()

    n = 3 * N_SEC
    return pl.pallas_call(
        body, name="grad_chip_exchange",
        out_shape=tuple(jax.ShapeDtypeStruct((3, rows, cols), BF16) for _, rows, cols in SECTIONS),
        in_specs=[ANY] * N_SEC, out_specs=(ANY,) * N_SEC,
        scratch_shapes=[pltpu.SemaphoreType.DMA((n,)), pltpu.SemaphoreType.DMA((n,))],
    )(*parts)


def _grad_finish(parts, far, chip):
    def body(chip_ref, *refs):
        p_refs, b_refs, g_refs = refs[:N_SEC], refs[N_SEC:2 * N_SEC], refs[2 * N_SEC:]
        for s in range(N_SEC):
            g = p_refs[s][0].astype(F32)
            for j in range(3):
                g = g + b_refs[s][j].astype(F32)
            g_refs[s][...] = g

    half = [(rows // 2, cols) for _, rows, cols in SECTIONS]
    return pl.pallas_call(
        body, name="grad_finish",
        out_shape=tuple(jax.ShapeDtypeStruct((rows, cols), F32) for _, rows, cols in SECTIONS),
        grid_spec=pltpu.PrefetchScalarGridSpec(
            num_scalar_prefetch=1, grid=(2,),
            in_specs=[pl.BlockSpec((1, r, c), lambda i, chip: (chip[0], i, 0)) for r, c in half]
            + [pl.BlockSpec((3, r, c), lambda i, chip: (0, i, 0)) for r, c in half],
            out_specs=tuple(pl.BlockSpec((r, c), lambda i, chip: (i, 0)) for r, c in half)),
        compiler_params=_params(dimension_semantics=("parallel",)),
    )(chip, *parts, *far)


def _sum_devices(parts, rows):
    tr = rows // 2

    def body(*refs):
        s = refs[0][...]
        for d in range(1, N_DEV):
            s = s + refs[d][...]
        refs[N_DEV][...] = s

    return pl.pallas_call(
        body, name="small_grad_sum",
        out_shape=jax.ShapeDtypeStruct((rows, LANES), F32),
        grid=(2,),
        in_specs=[pl.BlockSpec((tr, LANES), lambda i, d=d: (2 * d + i, 0)) for d in range(N_DEV)],
        out_specs=pl.BlockSpec((tr, LANES), lambda i: (i, 0)),
        compiler_params=_params(dimension_semantics=("parallel",)),
    )(*([parts] * N_DEV))


def _adamw(w, g, m, v, name):
    rows, cols = w.shape
    tr = rows
    while tr * cols * 4 > (1 << 20) and tr % 16 == 0:
        tr //= 2
    c1 = 1.0 / (1.0 - ADAM_B1 ** ADAM_STEP)
    c2 = 1.0 / (1.0 - ADAM_B2 ** ADAM_STEP)

    def body(w_ref, g_ref, m_ref, v_ref, d_ref, nm_ref, nv_ref):
        gv = g_ref[...]
        nm = ADAM_B1 * m_ref[...] + (1.0 - ADAM_B1) * gv
        nv = ADAM_B2 * v_ref[...] + (1.0 - ADAM_B2) * (gv * gv)
        nm_ref[...] = nm
        nv_ref[...] = nv
        d_ref[...] = (-ADAM_LR) * ((nm * c1) / (jnp.sqrt(nv * c2) + ADAM_EPS) + ADAM_WD * w_ref[...])

    spec = pl.BlockSpec((tr, cols), lambda i: (i, 0))
    shape = jax.ShapeDtypeStruct((rows, cols), F32)
    return pl.pallas_call(
        body, name=name,
        out_shape=(shape, shape, shape),
        grid=(rows // tr,),
        in_specs=[spec] * 4, out_specs=(spec,) * 3,
        compiler_params=_params(dimension_semantics=("parallel",)),
    )(w, g, m, v)


NAMES = ("ln1_g", "w_in", "b_in", "rpb", "w_att_o", "conv_w", "conv_b", "w_rg_a", "b_rg_a", "w_rg_i",
         "b_rg_i", "lru_lambda", "w_rec_o", "w_out", "ln2_g", "w_ff1", "w_ff2", "lnf_g")
TRANSPOSED = {"w_in": "w_in_t", "w_att_o": "w_att_o_t", "w_ff1": "w_ff1_t"}
ROW_SHARDED = ("w_rec_o", "w_out", "w_ff2")
REPLICATED = (("ln1_g", (1, D)), ("b_in", (1, D_IN)), ("rpb", (N_HEADS * N_RPB_R, N_RPB_C)),
              ("conv_b", (1, D_REC)), ("w_rg_a", (2 * N_REC_BLOCKS * REC_BLOCK, REC_BLOCK)),
              ("w_rg_i", (2 * N_REC_BLOCKS * REC_BLOCK, REC_BLOCK)), ("ln2_g", (1, D)), ("lnf_g", (1, D)))
SMALL_ROWS = 2160


def _chan_bits(vectors):
    chan = jnp.concatenate(vectors, axis=0)
    bits = lax.bitcast_convert_type(chan, BF16).reshape(-1)
    return jnp.pad(bits, (0, CHAN_BLOCK_ROWS * D - bits.shape[0])).reshape(CHAN_BLOCK_ROWS, D)


def _chan_from_bits(gathered):
    bits = gathered.reshape(N_DEV, CHAN_BLOCK_ROWS * D)[:, :2 * N_CHAN_ROWS * LANES]
    chan = lax.bitcast_convert_type(bits.reshape(N_DEV, N_CHAN_ROWS, LANES, 2), F32)
    return chan.transpose(1, 0, 2).reshape(N_CHAN_ROWS, D)


def kernel(x, ln1_g, w_in, b_in, rpb, w_att_o, conv_w, conv_b, w_rg_a, b_rg_a, w_rg_i, b_rg_i, lru_lambda, w_rec_o, w_out, ln2_g, w_ff1, w_ff2, lnf_g, loss_target, m_ln1_g, m_w_in, m_b_in, m_rpb, m_w_att_o, m_conv_w, m_conv_b, m_w_rg_a, m_b_rg_a, m_w_rg_i, m_b_rg_i, m_lru_lambda, m_w_rec_o, m_w_out, m_ln2_g, m_w_ff1, m_w_ff2, m_lnf_g, v_ln1_g, v_w_in, v_b_in, v_rpb, v_w_att_o, v_conv_w, v_conv_b, v_w_rg_a, v_b_rg_a, v_w_rg_i, v_b_rg_i, v_lru_lambda, v_w_rec_o, v_w_out, v_ln2_g, v_w_ff1, v_w_ff2, v_lnf_g):
    w = dict(zip(NAMES, (ln1_g, w_in, b_in, rpb, w_att_o, conv_w, conv_b, w_rg_a, b_rg_a, w_rg_i,
                         b_rg_i, lru_lambda, w_rec_o, w_out, ln2_g, w_ff1, w_ff2, lnf_g)))
    m = dict(zip(NAMES, (m_ln1_g, m_w_in, m_b_in, m_rpb, m_w_att_o, m_conv_w, m_conv_b, m_w_rg_a,
                         m_b_rg_a, m_w_rg_i, m_b_rg_i, m_lru_lambda, m_w_rec_o, m_w_out, m_ln2_g,
                         m_w_ff1, m_w_ff2, m_lnf_g)))
    v = dict(zip(NAMES, (v_ln1_g, v_w_in, v_b_in, v_rpb, v_w_att_o, v_conv_w, v_conv_b, v_w_rg_a,
                         v_b_rg_a, v_w_rg_i, v_b_rg_i, v_lru_lambda, v_w_rec_o, v_w_out, v_ln2_g,
                         v_w_ff1, v_w_ff2, v_lnf_g)))
    xi, yi, ci = _position()

    shard = {t: w[n][0].T.astype(BF16) for n, t in TRANSPOSED.items()}
    shard.update({n: w[n][0].astype(BF16) for n in ROW_SHARDED})
    shard["chan"] = _chan_bits([w[n][0] for n, _ in CHAN])
    gathered = dict(zip((n for n, _, _ in SECTIONS),
                        _all_gather([shard[n] for n, _, _ in SECTIONS], "weight_all_gather")))
    chan = _chan_from_bits(gathered.pop("chan"))
    p = dict(gathered)
    r0 = 0
    for n, rows in CHAN:
        p[n] = chan[r0:r0 + rows]
        r0 += rows
    p.update(ln1_g=w["ln1_g"], b_in=w["b_in"], rpb=w["rpb"][0], conv_b=w["conv_b"],
             w_rg_a=w["w_rg_a"][0], w_rg_i=w["w_rg_i"][0], ln2_g=w["ln2_g"],
             lnf_g=w["lnf_g"].reshape(1, D))

    loss_part, grad_x, grads = _local_step(x[0], loss_target[0], p)

    chan_g = jnp.concatenate([grads[n] for n, _ in CHAN], axis=0)
    chan_g = chan_g.reshape(N_CHAN_ROWS, N_DEV, LANES).transpose(1, 0, 2).astype(BF16)
    chan_g = jnp.pad(chan_g.reshape(N_DEV, -1), ((0, 0), (0, CHAN_BLOCK_ROWS * D - N_CHAN_ROWS * LANES)))
    grads["chan"] = chan_g.reshape(N_DEV * CHAN_BLOCK_ROWS, D)
    sect = [grads[n] for n, _, _ in SECTIONS]
    core = jnp.reshape(ci, (1,)).astype(jnp.int32)
    chip = jnp.reshape(2 * xi + yi, (1,)).astype(jnp.int32)
    parts = _pair_add(sect, _pair_exchange(sect), core)
    summed = dict(zip((n for n, _, _ in SECTIONS), _grad_finish(parts, _chip_exchange(parts), chip)))

    flat = jnp.concatenate([grads[n].reshape(-1) for n, _ in REPLICATED] + [loss_part.reshape(-1)])
    n_small = flat.shape[0]
    flat = jnp.pad(flat, (0, SMALL_ROWS * LANES - n_small)).reshape(SMALL_ROWS, LANES)
    (small_parts,) = _all_gather([flat], "small_grad_all_gather")
    small = _sum_devices(small_parts, SMALL_ROWS).reshape(-1)
    loss = small[n_small - 1]

    g, delta, new_m, new_v = {}, {}, {}, {}

    def update(n, g2, shape2):
        d2, m2, v2 = _adamw(w[n].reshape(shape2), g2, m[n].reshape(shape2), v[n].reshape(shape2),
                            "adamw_" + n)
        g[n], delta[n], new_m[n], new_v[n] = (a.reshape(w[n].shape) for a in (g2, d2, m2, v2))

    for n in ROW_SHARDED:
        update(n, summed[n], summed[n].shape)
    for n, t in TRANSPOSED.items():
        update(n, summed[t].T, summed[t].shape[::-1])
    chan_back = summed["chan"].reshape(-1)[:N_CHAN_ROWS * LANES].reshape(N_CHAN_ROWS, LANES)
    r0 = 0
    for n, rows in CHAN:
        update(n, chan_back[r0:r0 + rows], (rows, LANES))
        r0 += rows
    o = 0
    for n, shape2 in REPLICATED:
        size = shape2[0] * shape2[1]
        update(n, small[o:o + size].reshape(shape2), shape2)
        o += size

    return (loss, grad_x[None], *[g[n] for n in NAMES], *[delta[n] for n in NAMES],
            *[new_m[n] for n in NAMES], *[new_v[n] for n in NAMES])
```

```python
import math

import numpy as np
import jax
import jax.numpy as jnp
from jax import lax
from jax.experimental import pallas as pl
from jax.experimental.pallas import tpu as pltpu

F32 = jnp.float32
BF16 = jnp.bfloat16

T = 2048
D = 1024
D_ATT = 512
D_REC = 1024
D_FF = 4096
D_IN = 5632
N_HEADS = 8
DH = 64
GRID_W = 64
ROWS = T // GRID_W
WIN_H = 8
WIN_W = 16
KWIN = WIN_H * GRID_W
N_RPB_R = 2 * WIN_H - 1
N_RPB_C = 2 * WIN_W - 1
N_REC_BLOCKS = 16
REC_BLOCK = 64
CG = 128
N_CG = D_REC // CG
LRU_C = 8.0
EPS = 1e-6
N_DEV = 8
N_CHIPS = 4
LANES = 128

ADAM_LR = 0.001
ADAM_B1 = 0.9
ADAM_B2 = 0.999
ADAM_EPS = 1e-08
ADAM_WD = 0.01
ADAM_STEP = 10

MESH_AXES = ("x", "y", "c")
VMEM_LIMIT = 56 * 1024 * 1024

TILE = 512
DZ_SEGMENTS = ((0, 1), (1, 1), (2, 1), (3, 2), (5, 2), (7, 2), (9, 2))
N_DZ_TILES = D_IN // TILE


def _params(**kw):
    return pltpu.CompilerParams(vmem_limit_bytes=VMEM_LIMIT, **kw)


def _att_tables():
    rq = np.arange(2 * GRID_W) % GRID_W
    kc = np.arange(KWIN) % GRID_W
    win_start = np.clip(rq - WIN_W // 2, 0, GRID_W - WIN_W)
    valid = (kc[None, :] >= win_start[:, None]) & (kc[None, :] < win_start[:, None] + WIN_W)
    return valid.astype(np.float32), _pair_mask()


def _pair_mask():
    half = np.arange(2 * DH) // DH
    return (half[:, None] == half[None, :]).astype(np.float32)


def _dup_table():
    return np.concatenate([np.eye(REC_BLOCK, dtype=np.float32)] * 2, axis=1)


def _toeplitz_table():
    q = np.arange(GRID_W)[:, None]
    kc = np.arange(GRID_W)[None, :]
    dc = (kc - q + WIN_W - 1).reshape(-1)
    e = np.zeros((128, GRID_W * GRID_W), np.float32)
    ok = (dc >= 0) & (dc < N_RPB_C)
    e[dc[ok], np.arange(GRID_W * GRID_W)[ok]] = 1.0
    return e


def _row_shift_table():
    s = np.zeros((16, WIN_H * WIN_H), np.float32)
    for oi in range(WIN_H):
        for i in range(WIN_H):
            s[i - oi + WIN_H - 1, oi * WIN_H + i] = 1.0
    return s


def _sigmoid(x):
    return 0.5 * jnp.tanh(0.5 * x) + 0.5


def _softplus(x):
    return jnp.maximum(x, 0.0) + jnp.log(1.0 + jnp.exp(-jnp.abs(x)))


def _one_minus_square(log_a, a):
    x = 2.0 * log_a
    series = -x * (1.0 + x * (0.5 + x * (1.0 / 6.0)))
    return jnp.where(x > -0.02, series, 1.0 - a * a)


_GELU_C = math.sqrt(2.0 / math.pi)


def _gelu_and_grad(x):
    x2 = x * x
    inner = _GELU_C * (x + 0.044715 * x * x2)
    t = jnp.tanh(inner)
    g = 0.5 * x * (1.0 + t)
    dg = 0.5 * (1.0 + t) + 0.5 * x * (1.0 - t * t) * _GELU_C * (1.0 + 3.0 * 0.044715 * x2)
    return g, dg


def _dot(a, b):
    return jnp.dot(a, b, preferred_element_type=F32)


def _dot_nt(a, b):
    return lax.dot_general(a, b, (((1,), (1,)), ((), ())), preferred_element_type=F32)


def _dot_tn(a, b):
    return lax.dot_general(a, b, (((0,), (0,)), ((), ())), preferred_element_type=F32)


def _dot_exact(a, b):
    return jnp.dot(a, b, precision=lax.Precision.HIGHEST, preferred_element_type=F32)


def _shift_rows(x, s):
    n = x.shape[0]
    rows = lax.broadcasted_iota(jnp.int32, x.shape, 0)
    y = pltpu.roll(x, s % n, 0)
    if s > 0:
        return jnp.where(rows >= s, y, 0.0)
    return jnp.where(rows < n + s, y, 0.0)


def _rms_bwd(dh, xh, r, g):
    dxh = dh * g
    return r * (dxh - xh * jnp.mean(dxh * xh, axis=-1, keepdims=True))


def _matmul(a, b, mode, out_dtype, name, tm=512, tn=1024, tk=2048):
    if mode == "nn":
        (m, k), (k2, n) = a.shape, b.shape
    elif mode == "nt":
        (m, k), (n, k2) = a.shape, b.shape
    else:
        (k, m), (k2, n) = a.shape, b.shape
    assert k == k2
    tm, tn, tk = min(tm, m), min(tn, n), min(tk, k)
    assert m % tm == 0 and n % tn == 0 and k % tk == 0
    nk = k // tk
    dot = {"nn": _dot, "nt": _dot_nt, "tn": _dot_tn}[mode]

    def body(a_ref, b_ref, o_ref, acc):
        kk = pl.program_id(2)
        part = dot(a_ref[...].astype(BF16), b_ref[...].astype(BF16))
        if nk == 1:
            o_ref[...] = part.astype(out_dtype)
            return

        @pl.when(kk == 0)
        def _():
            acc[...] = part

        @pl.when(kk > 0)
        def _():
            acc[...] += part

        @pl.when(kk == nk - 1)
        def _():
            o_ref[...] = acc[...].astype(out_dtype)

    if mode == "tn":
        a_spec = pl.BlockSpec((tk, tm), lambda i, j, kk: (kk, i))
    else:
        a_spec = pl.BlockSpec((tm, tk), lambda i, j, kk: (i, kk))
    if mode == "nt":
        b_spec = pl.BlockSpec((tn, tk), lambda i, j, kk: (j, kk))
    else:
        b_spec = pl.BlockSpec((tk, tn), lambda i, j, kk: (kk, j))
    return pl.pallas_call(
        body, name=name,
        out_shape=jax.ShapeDtypeStruct((m, n), out_dtype),
        grid=(m // tm, n // tn, nk),
        in_specs=[a_spec, b_spec],
        out_specs=pl.BlockSpec((tm, tn), lambda i, j, kk: (i, j)),
        scratch_shapes=[pltpu.VMEM((tm, tn) if nk > 1 else (8, LANES), F32)],
        compiler_params=_params(dimension_semantics=("parallel", "parallel", "arbitrary")),
    )(a, b)


def _in_proj(x, g1, w_in_t, b_in):
    tm = 512

    def body(x_ref, g_ref, w_ref, b_ref, qkv_ref, uy_ref, gg_ref, h_ref, h_scr):
        j = pl.program_id(1)

        @pl.when(j == 0)
        def _():
            xv = x_ref[...]
            r = lax.rsqrt(jnp.mean(xv * xv, axis=-1, keepdims=True) + EPS)
            h = ((xv * r) * g_ref[...]).astype(BF16)
            h_scr[...] = h
            h_ref[...] = h

        z = _dot_nt(h_scr[...], w_ref[...]) + b_ref[...]

        @pl.when(j < 3)
        def _():
            qkv_ref[...] = z.astype(BF16)

        @pl.when((j >= 3) & (j < 7))
        def _():
            uy_ref[...] = z

        @pl.when(j >= 7)
        def _():
            gg_ref[...] = z

    return pl.pallas_call(
        body, name="in_proj",
        out_shape=(jax.ShapeDtypeStruct((T, 3 * D_ATT), BF16),
                   jax.ShapeDtypeStruct((T, 2 * D_REC), F32),
                   jax.ShapeDtypeStruct((T, 2 * D), F32),
                   jax.ShapeDtypeStruct((T, D), BF16)),
        grid=(T // tm, N_DZ_TILES),
        in_specs=[pl.BlockSpec((tm, D), lambda i, j: (i, 0)),
                  pl.BlockSpec((1, D), lambda i, j: (0, 0)),
                  pl.BlockSpec((TILE, D), lambda i, j: (j, 0)),
                  pl.BlockSpec((1, TILE), lambda i, j: (0, j))],
        out_specs=(pl.BlockSpec((tm, TILE), lambda i, j: (i, jnp.minimum(j, 2))),
                   pl.BlockSpec((tm, TILE), lambda i, j: (i, jnp.clip(j - 3, 0, 3))),
                   pl.BlockSpec((tm, TILE), lambda i, j: (i, jnp.clip(j - 7, 0, 3))),
                   pl.BlockSpec((tm, D), lambda i, j: (i, 0))),
        scratch_shapes=[pltpu.VMEM((tm, D), BF16)],
        compiler_params=_params(dimension_semantics=("parallel", "arbitrary")),
    )(x, g1, w_in_t, b_in)


def _segment_spec(rows, seg, row_index):
    off, n = seg
    if row_index:
        return pl.BlockSpec((rows, TILE), lambda i, j: (i, jnp.clip(j - off, 0, n - 1)))
    return pl.BlockSpec((rows, TILE), lambda j, kk: (kk, jnp.clip(j - off, 0, n - 1)))


def _dh_norm1_bwd(dz_segments, w_in_t, x, g1, dx1):
    tm = 512

    def body(*refs):
        seg_refs = refs[:7]
        w_ref, x_ref, g_ref, dx1_ref, gx_ref, dg_ref, acc = refs[7:]
        i, kk = pl.program_id(0), pl.program_id(1)

        @pl.when(kk == 0)
        def _():
            acc[...] = jnp.zeros_like(acc)

        for s, (off, n) in enumerate(DZ_SEGMENTS):
            @pl.when((kk >= off) & (kk < off + n))
            def _(s=s):
                acc[...] += _dot(seg_refs[s][...], w_ref[...])

        @pl.when((i == 0) & (kk == 0))
        def _():
            dg_ref[...] = jnp.zeros_like(dg_ref)

        @pl.when(kk == N_DZ_TILES - 1)
        def _():
            xv = x_ref[...]
            r = lax.rsqrt(jnp.mean(xv * xv, axis=-1, keepdims=True) + EPS)
            xh = xv * r
            dh = acc[...]
            dg_ref[...] += jnp.sum(dh * xh, axis=0, keepdims=True)
            gx_ref[...] = dx1_ref[...] + _rms_bwd(dh, xh, r, g_ref[...])

    tok = pl.BlockSpec((tm, D), lambda i, j: (i, 0))
    vec = pl.BlockSpec((1, D), lambda i, j: (0, 0))
    return pl.pallas_call(
        body, name="dh_norm1_bwd",
        out_shape=(jax.ShapeDtypeStruct((T, D), F32), jax.ShapeDtypeStruct((1, D), F32)),
        grid=(T // tm, N_DZ_TILES),
        in_specs=[_segment_spec(tm, seg, True) for seg in DZ_SEGMENTS]
        + [pl.BlockSpec((TILE, D), lambda i, j: (j, 0)), tok, vec, tok],
        out_specs=(tok, vec),
        scratch_shapes=[pltpu.VMEM((tm, D), F32)],
        compiler_params=_params(dimension_semantics=("arbitrary", "arbitrary")),
    )(*dz_segments, w_in_t, x, g1, dx1)


def _grad_w_in(dz_segments, h):
    tk = 1024
    nk = T // tk

    def body(*refs):
        seg_refs = refs[:7]
        h_ref, gw_ref, gb_ref, acc = refs[7:]
        j, kk = pl.program_id(0), pl.program_id(1)

        @pl.when(kk == 0)
        def _():
            acc[...] = jnp.zeros_like(acc)
            gb_ref[...] = jnp.zeros_like(gb_ref)

        for s, (off, n) in enumerate(DZ_SEGMENTS):
            @pl.when((j >= off) & (j < off + n))
            def _(s=s):
                a = seg_refs[s][...]
                acc[...] += _dot_tn(a, h_ref[...])
                gb_ref[...] += jnp.sum(a.astype(F32), axis=0, keepdims=True)

        @pl.when(kk == nk - 1)
        def _():
            gw_ref[...] = acc[...].astype(BF16)

    return pl.pallas_call(
        body, name="grad_w_in",
        out_shape=(jax.ShapeDtypeStruct((D_IN, D), BF16), jax.ShapeDtypeStruct((1, D_IN), F32)),
        grid=(N_DZ_TILES, nk),
        in_specs=[_segment_spec(tk, seg, False) for seg in DZ_SEGMENTS]
        + [pl.BlockSpec((tk, D), lambda j, kk: (kk, 0))],
        out_specs=(pl.BlockSpec((TILE, D), lambda j, kk: (j, 0)),
                   pl.BlockSpec((1, TILE), lambda j, kk: (0, j))),
        scratch_shapes=[pltpu.VMEM((TILE, D), F32)],
        compiler_params=_params(dimension_semantics=("parallel", "arbitrary")),
    )(*dz_segments, h)


def _rpb_expand(rpb):
    rpb2 = jnp.pad(rpb.reshape(N_HEADS * N_RPB_R, N_RPB_C), ((0, 0), (0, 128 - N_RPB_C)))
    table = jnp.asarray(_toeplitz_table())

    def body(r_ref, e_ref, o_ref):
        o_ref[...] = _dot_exact(r_ref[...], e_ref[...])

    tb = pl.pallas_call(
        body, name="rpb_expand",
        out_shape=jax.ShapeDtypeStruct((N_HEADS * N_RPB_R, GRID_W * GRID_W), F32),
    )(rpb2, table)
    tb = tb.reshape(N_HEADS, N_RPB_R, GRID_W, GRID_W)
    variants = []
    for oi in range(WIN_H):
        sl = tb[:, WIN_H - 1 - oi: 2 * WIN_H - 1 - oi]
        sl = sl.transpose(0, 2, 1, 3).reshape(N_HEADS // 2, 2 * GRID_W, KWIN)
        variants.append(sl)
    return jnp.stack(variants, axis=0)


def _rpb_reduce(gbias):
    g = gbias.reshape(N_HEADS // 2, WIN_H, 2, GRID_W, WIN_H, GRID_W)
    g = g.transpose(0, 2, 1, 4, 3, 5).reshape(N_HEADS, WIN_H * WIN_H, GRID_W * GRID_W)
    table = jnp.asarray(_toeplitz_table().T.copy())
    shift = jnp.asarray(_row_shift_table())

    def body(g_ref, e_ref, s_ref, o_ref):
        r = _dot_exact(g_ref[0], e_ref[...])
        o_ref[0] = _dot_exact(s_ref[...], r)

    out = pl.pallas_call(
        body, name="rpb_reduce",
        out_shape=jax.ShapeDtypeStruct((N_HEADS, 16, 128), F32),
        grid=(N_HEADS,),
        in_specs=[pl.BlockSpec((1, WIN_H * WIN_H, GRID_W * GRID_W), lambda h: (h, 0, 0)),
                  pl.BlockSpec((GRID_W * GRID_W, 128), lambda h: (0, 0)),
                  pl.BlockSpec((16, WIN_H * WIN_H), lambda h: (0, 0))],
        out_specs=pl.BlockSpec((1, 16, 128), lambda h: (h, 0, 0)),
        compiler_params=_params(dimension_semantics=("arbitrary",)),
    )(g, table, shift)
    return out[:, :N_RPB_R, :N_RPB_C]


def _att_scores(q_ref, k_ref, bias_ref, valid, hmask, r):
    rs = jnp.clip(r - WIN_H // 2, 0, ROWS - WIN_H)
    oi = r - rs
    q0 = pl.multiple_of(r * GRID_W, GRID_W)
    k0 = pl.multiple_of(rs * GRID_W, GRID_W)
    q_r = q_ref[pl.ds(q0, GRID_W), :]
    q2 = jnp.where(hmask, jnp.concatenate([q_r, q_r], axis=0), jnp.zeros((), BF16))
    kw = k_ref[pl.ds(k0, KWIN), :]
    s = _dot_nt(q2, kw) * (DH ** -0.5) + bias_ref[oi, 0]
    s = jnp.where(valid, s, -1e30)
    m = jnp.max(s, axis=-1, keepdims=True)
    p = jnp.exp(s - m)
    p = p / jnp.sum(p, axis=-1, keepdims=True)
    return p, q2, kw, q0, k0, oi


def _att_fwd(qkv, bias):
    valid_np, hmask_np = _att_tables()

    def body(q_ref, k_ref, v_ref, bias_ref, valid_ref, hmask_ref, o_ref):
        valid = valid_ref[...] > 0.5
        hmask = hmask_ref[...] > 0.5
        first_head = lax.broadcasted_iota(jnp.int32, (GRID_W, 2 * DH), 1) < DH

        def row(r, carry):
            p, _, _, q0, k0, _ = _att_scores(q_ref, k_ref, bias_ref, valid, hmask, r)
            o2 = _dot(p.astype(BF16), v_ref[pl.ds(k0, KWIN), :])
            o_ref[pl.ds(q0, GRID_W), :] = jnp.where(first_head, o2[:GRID_W], o2[GRID_W:]).astype(BF16)
            return carry

        lax.fori_loop(0, ROWS, row, 0, unroll=4)

    col = lambda off: pl.BlockSpec((T, 2 * DH), lambda hp: (0, hp + off))
    return pl.pallas_call(
        body, name="att_fwd",
        out_shape=jax.ShapeDtypeStruct((T, D_ATT), BF16),
        grid=(N_HEADS // 2,),
        in_specs=[col(0), col(4), col(8),
                  pl.BlockSpec((WIN_H, 1, 2 * GRID_W, KWIN), lambda hp: (0, hp, 0, 0)),
                  pl.BlockSpec((2 * GRID_W, KWIN), lambda hp: (0, 0)),
                  pl.BlockSpec((2 * DH, 2 * DH), lambda hp: (0, 0))],
        out_specs=pl.BlockSpec((T, 2 * DH), lambda hp: (0, hp)),
        compiler_params=_params(dimension_semantics=("parallel",)),
    )(qkv, qkv, qkv, bias, jnp.asarray(valid_np), jnp.asarray(hmask_np))


def _att_bwd(qkv, bias, datt, after):
    valid_np, hmask_np = _att_tables()

    def body(q_ref, k_ref, v_ref, do_ref, bias_ref, valid_ref, hmask_ref,
             dq_ref, dk_ref, dv_ref, gb_ref, dk_acc, dv_acc):
        valid = valid_ref[...] > 0.5
        hmask = hmask_ref[...] > 0.5
        first_head = lax.broadcasted_iota(jnp.int32, (GRID_W, 2 * DH), 1) < DH
        dk_acc[...] = jnp.zeros_like(dk_acc)
        dv_acc[...] = jnp.zeros_like(dv_acc)
        gb_ref[...] = jnp.zeros_like(gb_ref)

        def row(r, carry):
            p, q2, kw, q0, k0, oi = _att_scores(q_ref, k_ref, bias_ref, valid, hmask, r)
            do_r = do_ref[pl.ds(q0, GRID_W), :]
            do2 = jnp.where(hmask, jnp.concatenate([do_r, do_r], axis=0), jnp.zeros((), BF16))
            vw = v_ref[pl.ds(k0, KWIN), :]
            dp = _dot_nt(do2, vw)
            ds = p * (dp - jnp.sum(dp * p, axis=-1, keepdims=True))
            p16 = p.astype(BF16)
            ds16 = ds.astype(BF16)
            dv_acc[pl.ds(k0, KWIN), :] += _dot_tn(p16, do2)
            dk_acc[pl.ds(k0, KWIN), :] += _dot_tn(ds16, q2) * (DH ** -0.5)
            dq2 = _dot(ds16, kw) * (DH ** -0.5)
            dq_ref[pl.ds(q0, GRID_W), :] = jnp.where(first_head, dq2[:GRID_W], dq2[GRID_W:]).astype(BF16)
            gb_ref[0, oi] += ds
            return carry

        lax.fori_loop(0, ROWS, row, 0, unroll=4)
        dk_ref[...] = dk_acc[...].astype(BF16)
        dv_ref[...] = dv_acc[...].astype(BF16)

    col = lambda off: pl.BlockSpec((T, 2 * DH), lambda hp: (0, hp + off))
    out_col = pl.BlockSpec((T, 2 * DH), lambda hp: (0, hp))
    return pl.pallas_call(
        body, name="att_bwd",
        out_shape=(jax.ShapeDtypeStruct((T, D_ATT), BF16),) * 3
        + (jax.ShapeDtypeStruct((N_HEADS // 2, WIN_H, 2 * GRID_W, KWIN), F32),),
        grid=(N_HEADS // 2,),
        in_specs=[col(0), col(4), col(8), col(0),
                  pl.BlockSpec((WIN_H, 1, 2 * GRID_W, KWIN), lambda hp: (0, hp, 0, 0)),
                  pl.BlockSpec((2 * GRID_W, KWIN), lambda hp: (0, 0)),
                  pl.BlockSpec((2 * DH, 2 * DH), lambda hp: (0, 0))],
        out_specs=(out_col, out_col, out_col,
                   pl.BlockSpec((1, WIN_H, 2 * GRID_W, KWIN), lambda hp: (hp, 0, 0, 0))),
        scratch_shapes=[pltpu.VMEM((T, 2 * DH), F32), pltpu.VMEM((T, 2 * DH), F32)],
        compiler_params=_params(dimension_semantics=("parallel",)),
    )(qkv, qkv, qkv, datt, bias, jnp.asarray(valid_np) + after, jnp.asarray(hmask_np))


def _conv_taps(up):
    return (_shift_rows(up, 2), _shift_rows(up, 1), up, _shift_rows(up, -1))


def _pair_block_diag(w_pair, dup, same_half):
    return jnp.where(same_half, _dot(w_pair.astype(BF16), dup), 0.0).astype(BF16)


def _gates(u, u16, wa, ba, wi, bi, lam):
    r = _sigmoid(_dot(u16, wa) + ba)
    ig = _sigmoid(_dot(u16, wi) + bi)
    sp = _softplus(-lam)
    log_a = (-LRU_C) * r * sp
    a = jnp.exp(log_a)
    mult = jnp.sqrt(jnp.maximum(_one_minus_square(log_a, a), 0.0))
    return r, ig, sp, a, mult


SCAN_BLOCKS = 2


def _scans(jobs):
    c = jobs[0][0].shape[1]
    nblk = T // 8
    rows = lax.broadcasted_iota(jnp.int32, (8, c), 0)

    def block(a, b, reverse):
        for s in (1, 2, 4):
            if reverse:
                keep = rows < 8 - s
                a_s = jnp.where(keep, pltpu.roll(a, 8 - s, 0), 1.0)
                b_s = jnp.where(keep, pltpu.roll(b, 8 - s, 0), 0.0)
            else:
                keep = rows >= s
                a_s = jnp.where(keep, pltpu.roll(a, s, 0), 1.0)
                b_s = jnp.where(keep, pltpu.roll(b, s, 0), 0.0)
            b = a * b_s + b
            a = a * a_s
        return a, b

    def step(i, carry):
        out = []
        for (a_ref, b_ref, h_ref, reverse), h_prev in zip(jobs, carry):
            for u in range(SCAN_BLOCKS):
                blk = i * SCAN_BLOCKS + u
                if reverse:
                    blk = nblk - 1 - blk
                t0 = pl.multiple_of(blk * 8, 8)
                a, b = block(a_ref[pl.ds(t0, 8), :], b_ref[pl.ds(t0, 8), :], reverse)
                h = a * h_prev + b
                h_ref[pl.ds(t0, 8), :] = h
                h_prev = jnp.broadcast_to(h[0:1] if reverse else h[7:8], (8, c))
            out.append(h_prev)
        return tuple(out)

    lax.fori_loop(0, nblk // SCAN_BLOCKS, step, tuple(jnp.zeros((8, c), F32) for _ in jobs))


def _rec_specs():
    tok = lambda off: pl.BlockSpec((T, CG), lambda g: (0, g + off))
    per_ch = lambda rows: pl.BlockSpec((rows, CG), lambda g: (0, g))
    wspec = pl.BlockSpec((2, 1, CG, REC_BLOCK), lambda g: (0, g, 0, 0))
    const = lambda shape: pl.BlockSpec(shape, lambda g: (0, 0))
    return tok, per_ch, wspec, const


def _rec_fwd(uy, conv_w, conv_b, w_a, b_a, w_i, b_i, lam):
    tok, per_ch, wspec, const = _rec_specs()

    def body(up_ref, yb_ref, cw_ref, cb_ref, wa_ref, ba_ref, wi_ref, bi_ref, lam_ref, dup_ref, half_ref,
             hf_ref, hb_ref, yrec_ref, a_f, bx_f, a_b, bx_b):
        dup = dup_ref[...]
        same_half = half_ref[...] > 0.5
        taps = _conv_taps(up_ref[...])
        u = cb_ref[...]
        for j in range(4):
            u = u + taps[j] * cw_ref[j:j + 1, :]
        u16 = u.astype(BF16)
        for d, (a_s, bx_s) in enumerate(((a_f, bx_f), (a_b, bx_b))):
            wa = _pair_block_diag(wa_ref[d, 0], dup, same_half)
            wi = _pair_block_diag(wi_ref[d, 0], dup, same_half)
            _, ig, _, a, mult = _gates(u, u16, wa, ba_ref[d:d + 1, :], wi, bi_ref[d:d + 1, :],
                                       lam_ref[d:d + 1, :])
            a_s[...] = a
            bx_s[...] = mult * (ig * u)
        _scans([(a_f, bx_f, hf_ref, False), (a_b, bx_b, hb_ref, True)])
        gelu, _ = _gelu_and_grad(yb_ref[...])
        yrec_ref[...] = ((hf_ref[...] + hb_ref[...]) * gelu).astype(BF16)

    return pl.pallas_call(
        body, name="rec_fwd",
        out_shape=(jax.ShapeDtypeStruct((T, D_REC), F32), jax.ShapeDtypeStruct((T, D_REC), F32),
                   jax.ShapeDtypeStruct((T, D_REC), BF16)),
        grid=(N_CG,),
        in_specs=[tok(0), tok(N_CG), per_ch(4), per_ch(1), wspec, per_ch(2), wspec, per_ch(2), per_ch(2),
                  const((REC_BLOCK, CG)), const((CG, CG))],
        out_specs=(tok(0), tok(0), tok(0)),
        scratch_shapes=[pltpu.VMEM((T, CG), F32)] * 4,
        compiler_params=_params(dimension_semantics=("parallel",)),
    )(uy, uy, conv_w, conv_b, w_a, b_a, w_i, b_i, lam,
      jnp.asarray(_dup_table(), BF16), jnp.asarray(_pair_mask()))


def _rec_bwd(uy, hf, hb, dyrec, conv_w, conv_b, w_a, b_a, w_i, b_i, lam):
    tok, per_ch, wspec, const = _rec_specs()

    def body(up_ref, yb_ref, hf_ref, hb_ref, dy_ref, cw_ref, cb_ref, wa_ref, ba_ref, wi_ref, bi_ref,
             lam_ref, dup_ref, dupt_ref, half_ref,
             dup_out, dyb_ref, dcw_ref, dcb_ref, dwa_ref, dba_ref, dwi_ref, dbi_ref, dlam_ref,
             a_s0, a_s1, dh_s, g_s0, g_s1):
        dup = dup_ref[...]
        dup_t = dupt_ref[...]
        same_half = half_ref[...] > 0.5
        taps = _conv_taps(up_ref[...])
        u = cb_ref[...]
        for j in range(4):
            u = u + taps[j] * cw_ref[j:j + 1, :]
        u16 = u.astype(BF16)
        gelu, dgelu = _gelu_and_grad(yb_ref[...])
        dy = dy_ref[...]
        dyb_ref[...] = (dy * (hf_ref[...] + hb_ref[...]) * dgelu).astype(BF16)
        dh_s[...] = dy * gelu
        gate_values = []
        for d, a_s in enumerate((a_s0, a_s1)):
            wa = _pair_block_diag(wa_ref[d, 0], dup, same_half)
            wi = _pair_block_diag(wi_ref[d, 0], dup, same_half)
            lam_d = lam_ref[d:d + 1, :]
            r, ig, sp, a, mult = _gates(u, u16, wa, ba_ref[d:d + 1, :], wi, bi_ref[d:d + 1, :], lam_d)
            a_s[...] = _shift_rows(a, 1 if d == 1 else -1)
            gate_values.append((wa, wi, lam_d, r, ig, sp, a, mult))
        _scans([(a_s0, dh_s, g_s0, True), (a_s1, dh_s, g_s1, False)])
        du = jnp.zeros((T, CG), F32)
        for d, g_s in enumerate((g_s0, g_s1)):
            reverse = d == 1
            wa, wi, lam_d, r, ig, sp, a, mult = gate_values[d]
            g = g_s[...]
            h_prev = _shift_rows(hb_ref[...], -1) if reverse else _shift_rows(hf_ref[...], 1)
            da = g * h_prev
            dmult = g * (ig * u)
            dig = g * mult * u
            du = du + g * mult * ig
            dmult_dlog = jnp.where(mult > 0.0, -(a * a) / mult, 0.0)
            dlog_a = da * a + dmult * dmult_dlog
            dr = dlog_a * ((-LRU_C) * sp)
            dsp = jnp.sum(dlog_a * ((-LRU_C) * r), axis=0, keepdims=True)
            dlam_ref[d:d + 1, :] = dsp * (-_sigmoid(-lam_d))
            dga = dr * r * (1.0 - r)
            dgi = dig * ig * (1.0 - ig)
            dga16 = dga.astype(BF16)
            dgi16 = dgi.astype(BF16)
            du = du + _dot_nt(dga16, wa) + _dot_nt(dgi16, wi)
            dwa_ref[d, 0] = _dot_exact(jnp.where(same_half, _dot_tn(u16, dga16), 0.0), dup_t)
            dwi_ref[d, 0] = _dot_exact(jnp.where(same_half, _dot_tn(u16, dgi16), 0.0), dup_t)
            dba_ref[d:d + 1, :] = jnp.sum(dga, axis=0, keepdims=True)
            dbi_ref[d:d + 1, :] = jnp.sum(dgi, axis=0, keepdims=True)
        dcb_ref[...] = jnp.sum(du, axis=0, keepdims=True)
        for j in range(4):
            dcw_ref[j:j + 1, :] = jnp.sum(du * taps[j], axis=0, keepdims=True)
        dup_in = (_shift_rows(du, -2) * cw_ref[0:1, :] + _shift_rows(du, -1) * cw_ref[1:2, :]
                  + du * cw_ref[2:3, :] + _shift_rows(du, 1) * cw_ref[3:4, :])
        dup_out[...] = dup_in.astype(BF16)

    wshape = jax.ShapeDtypeStruct((2, N_CG, CG, REC_BLOCK), F32)
    vec = lambda rows: jax.ShapeDtypeStruct((rows, D_REC), F32)
    dup_np = _dup_table()
    return pl.pallas_call(
        body, name="rec_bwd",
        out_shape=(jax.ShapeDtypeStruct((T, D_REC), BF16), jax.ShapeDtypeStruct((T, D_REC), BF16),
                   vec(4), vec(1), wshape, vec(2), wshape, vec(2), vec(2)),
        grid=(N_CG,),
        in_specs=[tok(0), tok(N_CG), tok(0), tok(0), tok(0),
                  per_ch(4), per_ch(1), wspec, per_ch(2), wspec, per_ch(2), per_ch(2),
                  const((REC_BLOCK, CG)), const((CG, REC_BLOCK)), const((CG, CG))],
        out_specs=(tok(0), tok(0), per_ch(4), per_ch(1), wspec, per_ch(2), wspec, per_ch(2), per_ch(2)),
        scratch_shapes=[pltpu.VMEM((T, CG), F32)] * 5,
        compiler_params=_params(dimension_semantics=("parallel",)),
    )(uy, uy, hf, hb, dyrec, conv_w, conv_b, w_a, b_a, w_i, b_i, lam,
      jnp.asarray(dup_np, BF16), jnp.asarray(dup_np.T.copy()), jnp.asarray(_pair_mask()))


TM_MIX = 256


def _mix_specs():
    tok = lambda width, blk=0: pl.BlockSpec((TM_MIX, width), lambda i: (i, blk))
    full = lambda shape: pl.BlockSpec(shape, lambda i: (0, 0))
    return tok, full


def _mix_fwd(x, att, yrec, gg, w_att_o_t, w_rec_o, w_out):
    tok, full = _mix_specs()

    def body(x_ref, att_ref, yr_ref, ga_ref, gr_ref, wao_ref, wro_ref, wo_ref, x1_ref, mixed_ref):
        y_att = _dot_nt(att_ref[...], wao_ref[...])
        y_rec = _dot(yr_ref[...], wro_ref[...])
        mixed = (_sigmoid(ga_ref[...]) * y_att + _sigmoid(gr_ref[...]) * y_rec).astype(BF16)
        mixed_ref[...] = mixed
        x1_ref[...] = x_ref[...] + _dot(mixed, wo_ref[...])

    return pl.pallas_call(
        body, name="mix_fwd",
        out_shape=(jax.ShapeDtypeStruct((T, D), F32), jax.ShapeDtypeStruct((T, D), BF16)),
        grid=(T // TM_MIX,),
        in_specs=[tok(D), tok(D_ATT), tok(D_REC), tok(D, 0), tok(D, 1),
                  full((D, D_ATT)), full((D_REC, D)), full((D, D))],
        out_specs=(tok(D), tok(D)),
        compiler_params=_params(dimension_semantics=("parallel",)),
    )(x, att, yrec, gg, gg, w_att_o_t, w_rec_o, w_out)


def _mix_bwd(dx1, att, yrec, gg, w_att_o_t, w_rec_o, w_out):
    tok, full = _mix_specs()

    def body(dx_ref, att_ref, yr_ref, ga_ref, gr_ref, wao_ref, wro_ref, wo_ref,
             dga_ref, dgr_ref, dya_ref, dyr_ref, datt_ref, dyrp_ref):
        dmixed = _dot_nt(dx_ref[...].astype(BF16), wo_ref[...])
        y_att = _dot_nt(att_ref[...], wao_ref[...])
        y_rec = _dot(yr_ref[...], wro_ref[...])
        sa = _sigmoid(ga_ref[...])
        sr = _sigmoid(gr_ref[...])
        dga_ref[...] = (dmixed * y_att * sa * (1.0 - sa)).astype(BF16)
        dgr_ref[...] = (dmixed * y_rec * sr * (1.0 - sr)).astype(BF16)
        dya = (dmixed * sa).astype(BF16)
        dyr = (dmixed * sr).astype(BF16)
        dya_ref[...] = dya
        dyr_ref[...] = dyr
        datt_ref[...] = _dot(dya, wao_ref[...]).astype(BF16)
        dyrp_ref[...] = _dot_nt(dyr, wro_ref[...])

    return pl.pallas_call(
        body, name="mix_bwd",
        out_shape=(jax.ShapeDtypeStruct((T, D), BF16), jax.ShapeDtypeStruct((T, D), BF16),
                   jax.ShapeDtypeStruct((T, D), BF16), jax.ShapeDtypeStruct((T, D), BF16),
                   jax.ShapeDtypeStruct((T, D_ATT), BF16), jax.ShapeDtypeStruct((T, D_REC), F32)),
        grid=(T // TM_MIX,),
        in_specs=[tok(D), tok(D_ATT), tok(D_REC), tok(D, 0), tok(D, 1),
                  full((D, D_ATT)), full((D_REC, D)), full((D, D))],
        out_specs=(tok(D), tok(D), tok(D), tok(D), tok(D_ATT), tok(D_REC)),
        compiler_params=_params(dimension_semantics=("parallel",)),
    )(dx1, att, yrec, gg, gg, w_att_o_t, w_rec_o, w_out)


TM_FFN = 256
FF_CHUNK = 1024


def _ffn_loss(x1, target, g2, gf, w_ff1_t, w_ff2):
    n_chunks = D_FF // FF_CHUNK

    def body(x1_ref, tg_ref, g2_ref, gf_ref, w1_hbm, w2_hbm,
             loss_ref, dx1_ref, h2_ref, act_ref, dpre_ref, dx2_ref, dg2_ref, dgf_ref,
             w1, w2, relu_s):
        i = pl.program_id(0)

        @pl.when(i == 0)
        def _():
            pltpu.sync_copy(w1_hbm, w1)
            pltpu.sync_copy(w2_hbm, w2)
            loss_ref[...] = jnp.zeros_like(loss_ref)
            dg2_ref[...] = jnp.zeros_like(dg2_ref)
            dgf_ref[...] = jnp.zeros_like(dgf_ref)

        x1v = x1_ref[...]
        r2 = lax.rsqrt(jnp.mean(x1v * x1v, axis=-1, keepdims=True) + EPS)
        xh2 = x1v * r2
        h2 = (xh2 * g2_ref[...]).astype(BF16)
        h2_ref[...] = h2
        x2 = x1v
        for c in range(n_chunks):
            ff = slice(c * FF_CHUNK, (c + 1) * FF_CHUNK)
            rl = jnp.maximum(_dot_nt(h2, w1[ff, :]), 0.0)
            relu_s[:, ff] = rl
            act = (rl * rl).astype(BF16)
            act_ref[:, ff] = act
            x2 = x2 + _dot(act, w2[ff, :])
        r3 = lax.rsqrt(jnp.mean(x2 * x2, axis=-1, keepdims=True) + EPS)
        xh3 = x2 * r3
        err = xh3 * gf_ref[...] - tg_ref[...]
        loss_ref[...] += 0.5 * jnp.sum(jnp.mean(err * err, axis=-1, keepdims=True))
        dy = err * (1.0 / D)
        dgf_ref[...] += jnp.sum(dy * xh3, axis=0, keepdims=True)
        dx2 = _rms_bwd(dy, xh3, r3, gf_ref[...])
        dx2_16 = dx2.astype(BF16)
        dx2_ref[...] = dx2_16
        dh2 = jnp.zeros((TM_FFN, D), F32)
        for c in range(n_chunks):
            ff = slice(c * FF_CHUNK, (c + 1) * FF_CHUNK)
            dpre = (_dot_nt(dx2_16, w2[ff, :]) * (2.0 * relu_s[:, ff])).astype(BF16)
            dpre_ref[:, ff] = dpre
            dh2 = dh2 + _dot(dpre, w1[ff, :])
        dg2_ref[...] += jnp.sum(dh2 * xh2, axis=0, keepdims=True)
        dx1_ref[...] = dx2 + _rms_bwd(dh2, xh2, r2, g2_ref[...])

    tok = lambda width: pl.BlockSpec((TM_FFN, width), lambda i: (i, 0))
    vec = pl.BlockSpec((1, D), lambda i: (0, 0))
    hbm = pl.BlockSpec(memory_space=pl.ANY)
    return pl.pallas_call(
        body, name="ffn_loss",
        out_shape=(jax.ShapeDtypeStruct((8, 128), F32), jax.ShapeDtypeStruct((T, D), F32),
                   jax.ShapeDtypeStruct((T, D), BF16), jax.ShapeDtypeStruct((T, D_FF), BF16),
                   jax.ShapeDtypeStruct((T, D_FF), BF16), jax.ShapeDtypeStruct((T, D), BF16),
                   jax.ShapeDtypeStruct((1, D), F32), jax.ShapeDtypeStruct((1, D), F32)),
        grid=(T // TM_FFN,),
        in_specs=[tok(D), tok(D), vec, vec, hbm, hbm],
        out_specs=(pl.BlockSpec((8, 128), lambda i: (0, 0)), tok(D), tok(D), tok(D_FF), tok(D_FF), tok(D),
                   vec, vec),
        scratch_shapes=[pltpu.VMEM((D_FF, D), BF16), pltpu.VMEM((D_FF, D), BF16),
                        pltpu.VMEM((TM_FFN, D_FF), F32)],
        compiler_params=_params(dimension_semantics=("arbitrary",)),
    )(x1, target, g2, gf, w_ff1_t, w_ff2)


def _local_step(x, target, p, reduce_early):
    bias = _rpb_expand(p["rpb"])
    pairs = lambda w: w.reshape(2, N_CG, CG, REC_BLOCK)
    w_a, w_i = pairs(p["w_rg_a"]), pairs(p["w_rg_i"])
    rec_params = (p["conv_w"], p["conv_b"], w_a, p["b_rg_a"], w_i, p["b_rg_i"], p["lru_lambda"])

    qkv, uy, gg, h = _in_proj(x, p["ln1_g"], p["w_in_t"], p["b_in"])
    att = _att_fwd(qkv, bias)
    hf, hb, yrec = _rec_fwd(uy, *rec_params)
    x1, mixed = _mix_fwd(x, att, yrec, gg, p["w_att_o_t"], p["w_rec_o"], p["w_out"])
    loss8, dx1, h2, act, dpre, dx2, g_ln2, g_lnf = _ffn_loss(
        x1, target, p["ln2_g"], p["lnf_g"], p["w_ff1_t"], p["w_ff2"])

    dga, dgr, dya, dyr, datt, dyrp = _mix_bwd(dx1, att, yrec, gg, p["w_att_o_t"], p["w_rec_o"], p["w_out"])
    dup, dyb, g_cw, g_cb, g_wa, g_ba, g_wi, g_bi, g_lam = _rec_bwd(uy, hf, hb, dyrp, *rec_params)
    blocks = lambda g: g.reshape(2, N_REC_BLOCKS, REC_BLOCK, REC_BLOCK)
    grads = {
        "w_att_o_t": _matmul(dya, att, "tn", BF16, "g_w_att_o"),
        "conv_w": g_cw, "conv_b": g_cb, "w_rg_a": blocks(g_wa), "b_rg_a": g_ba,
        "w_rg_i": blocks(g_wi), "b_rg_i": g_bi, "lru_lambda": g_lam,
        "w_rec_o": _matmul(yrec, dyr, "tn", BF16, "g_w_rec_o"),
        "w_out": _matmul(mixed, dx1, "tn", BF16, "g_w_out"),
        "ln2_g": g_ln2,
        "w_ff1_t": _matmul(dpre, h2, "tn", BF16, "g_w_ff1"),
        "w_ff2": _matmul(act, dx2, "tn", BF16, "g_w_ff2"),
        "lnf_g": g_lnf,
    }
    after = reduce_early(grads)
    dq, dk, dv, gbias = _att_bwd(qkv, bias, datt, after)
    dz = (dq, dk, dv, dup, dyb, dga, dgr)
    grad_x, g_ln1 = _dh_norm1_bwd(dz, p["w_in_t"], x, p["ln1_g"], dx1)
    g_w_in_t, g_b_in = _grad_w_in(dz, h)
    grads.update(ln1_g=g_ln1, w_in_t=g_w_in_t, b_in=g_b_in, rpb=_rpb_reduce(gbias))
    return loss8[0:1, 0:1], grad_x, grads


MESH_ID = pl.DeviceIdType.MESH
ANY = pl.BlockSpec(memory_space=pl.ANY)

CHAN_BLOCK_ROWS = 32
SECTIONS = (("w_in_t", 704, D), ("w_rec_o", 128, D), ("w_out", 128, D), ("w_ff1_t", 512, D),
            ("w_ff2", 512, D), ("chan", CHAN_BLOCK_ROWS, D), ("w_att_o_t", 128, D_ATT))
N_SEC = len(SECTIONS)
N_CHAN_ROWS = 10
CHAN = (("conv_w", 4), ("b_rg_a", 2), ("b_rg_i", 2), ("lru_lambda", 2))


def _position():
    return lax.axis_index("x"), lax.axis_index("y"), lax.axis_index("c")


def _other_chips(x, y):
    return [(1 - x, y), (x, 1 - y), (1 - x, 1 - y)]


def _block_of(ref, dev, rows):
    return ref.at[pl.ds(pl.multiple_of(dev * rows, 16), rows)]


def _all_gather(shards, name):
    ns = len(shards)

    def body(*refs):
        x_refs, out_refs = refs[:ns], refs[ns:2 * ns]
        send_sems, recv_sems, local_sems = refs[2 * ns:]
        x, y, c = _position()
        me, sibling = (x, y, c), (x, y, 1 - c)
        chips = _other_chips(x, y)

        def rows(s, px, py, pc):
            return _block_of(out_refs[s], 4 * px + 2 * py + pc, shards[s].shape[0])

        def copy(k, s, block, to, from_shard=False):
            return pltpu.make_async_remote_copy(
                src_ref=x_refs[s] if from_shard else rows(s, *block), dst_ref=rows(s, *block),
                send_sem=send_sems.at[k * ns + s], recv_sem=recv_sems.at[k * ns + s],
                device_id=to, device_id_type=MESH_ID)

        sections = range(ns)
        mine = [pltpu.make_async_copy(x_refs[s], rows(s, *me), local_sems.at[s]) for s in sections]
        first = [copy(0, s, me, sibling, True) for s in sections]
        first += [copy(1 + j, s, me, (*chip, c), True) for j, chip in enumerate(chips) for s in sections]
        for cp in mine + first:
            cp.start()
        passed = []
        for j, chip in enumerate(chips):
            for s in sections:
                copy(1 + j, s, (*chip, c), me).wait_recv()
                passed.append(copy(4 + j, s, (*chip, c), sibling))
                passed[-1].start()
        for s in sections:
            copy(0, s, sibling, me).wait_recv()
        for j, chip in enumerate(chips):
            for s in sections:
                copy(4 + j, s, (*chip, 1 - c), me).wait_recv()
        for cp in first + passed:
            cp.wait_send()
        for cp in mine:
            cp.wait()

    return pl.pallas_call(
        body, name=name,
        out_shape=tuple(jax.ShapeDtypeStruct((N_DEV * s.shape[0], s.shape[1]), s.dtype) for s in shards),
        in_specs=[ANY] * ns, out_specs=(ANY,) * ns,
        scratch_shapes=[pltpu.SemaphoreType.DMA((7 * ns,)), pltpu.SemaphoreType.DMA((7 * ns,)),
                        pltpu.SemaphoreType.DMA((ns,))],
    )(*shards)


def _pair_exchange(sections, grads, name):
    ns = len(sections)

    def body(*refs):
        g_refs, land = refs[:ns], refs[ns:2 * ns]
        send_sems, recv_sems = refs[2 * ns:]
        x, y, c = _position()
        copies = [pltpu.make_async_remote_copy(
            src_ref=_block_of(g_refs[s], 2 * k + 1 - c, rows), dst_ref=land[s].at[k],
            send_sem=send_sems.at[k * ns + s], recv_sem=recv_sems.at[k * ns + s],
            device_id=(x, y, 1 - c), device_id_type=MESH_ID)
            for k in range(N_CHIPS) for s, (_, rows, _) in enumerate(sections)]
        for cp in copies:
            cp.start()
        for cp in copies:
            cp.wait_recv()
        for cp in copies:
            cp.wait_send()

    n = N_CHIPS * ns
    return pl.pallas_call(
        body, name=name,
        out_shape=tuple(jax.ShapeDtypeStruct((N_CHIPS, rows, cols), BF16) for _, rows, cols in sections),
        in_specs=[ANY] * ns, out_specs=(ANY,) * ns,
        scratch_shapes=[pltpu.SemaphoreType.DMA((n,)), pltpu.SemaphoreType.DMA((n,))],
    )(*grads)


def _pair_add(sections, grads, got, core, name):
    ns = len(sections)

    def body(core_ref, *refs):
        g_refs, got_refs, p_refs = refs[:ns], refs[ns:2 * ns], refs[2 * ns:]
        for s in range(ns):
            p_refs[s][0] = (g_refs[s][...].astype(F32) + got_refs[s][0].astype(F32)).astype(BF16)

    slot = [pl.BlockSpec((1, rows, cols), lambda k, c: (k, 0, 0)) for _, rows, cols in sections]
    return pl.pallas_call(
        body, name=name,
        out_shape=tuple(jax.ShapeDtypeStruct((N_CHIPS, rows, cols), BF16) for _, rows, cols in sections),
        grid_spec=pltpu.PrefetchScalarGridSpec(
            num_scalar_prefetch=1, grid=(N_CHIPS,),
            in_specs=[pl.BlockSpec((rows, cols), lambda k, c: (2 * k + c[0], 0)) for _, rows, cols in sections]
            + slot,
            out_specs=tuple(slot)),
        compiler_params=_params(dimension_semantics=("parallel",)),
    )(core, *grads, *got)


def _chip_copies(sections, p_refs, land, send_sems, recv_sems):
    ns = len(sections)
    x, y, c = _position()
    return [pltpu.make_async_remote_copy(
        src_ref=p_refs[s].at[2 * cx + cy], dst_ref=land[s].at[j],
        send_sem=send_sems.at[j * ns + s], recv_sem=recv_sems.at[j * ns + s],
        device_id=(cx, cy, c), device_id_type=MESH_ID)
        for j, (cx, cy) in enumerate(_other_chips(x, y)) for s in range(ns)]


def _chip_exchange(sections, parts, name):
    ns = len(sections)

    def body(*refs):
        copies = _chip_copies(sections, refs[:ns], refs[ns:2 * ns], *refs[2 * ns:])
        for cp in copies:
            cp.start()
        for cp in copies:
            cp.wait_recv()
        for cp in copies:
            cp.wait_send()

    n = 3 * ns
    return pl.pallas_call(
        body, name=name,
        out_shape=tuple(jax.ShapeDtypeStruct((3, rows, cols), BF16) for _, rows, cols in sections),
        in_specs=[ANY] * ns, out_specs=(ANY,) * ns,
        scratch_shapes=[pltpu.SemaphoreType.DMA((n,)), pltpu.SemaphoreType.DMA((n,))],
    )(*parts)


HBM = pl.BlockSpec(memory_space=pltpu.HBM)
SEM = pl.BlockSpec(memory_space=pltpu.SEMAPHORE)
EFFECT = pltpu.SideEffectType.DATAFLOW_SIDE_EFFECTING


def _chip_exchange_start(sections, parts, name):
    ns = len(sections)

    def body(*refs):
        p_refs, land = refs[:ns], refs[ns:2 * ns]
        send_sems, recv_sems = refs[2 * ns], refs[2 * ns + 1]
        token = refs[-1]
        for cp in _chip_copies(sections, p_refs, land, send_sems, recv_sems):
            cp.start()
        token[...] = jnp.zeros_like(token)

    in_hbm = lambda a: pltpu.with_memory_space_constraint(a, pltpu.HBM)
    zones = [lax.empty((3, rows, cols), BF16) for _, rows, cols in sections]
    out = pl.pallas_call(
        body, name=name,
        out_shape=(pltpu.SemaphoreType.DMA((3 * ns,)), pltpu.SemaphoreType.DMA((3 * ns,)),
                   *[pltpu.HBM(a.shape, a.dtype) for a in parts], *[pltpu.HBM(a.shape, a.dtype) for a in zones],
                   jax.ShapeDtypeStruct((8, LANES), F32)),
        in_specs=[HBM] * (2 * ns),
        out_specs=(SEM, SEM, *[HBM] * (2 * ns), pl.BlockSpec(memory_space=pltpu.VMEM)),
        input_output_aliases={i: 2 + i for i in range(2 * ns)},
        compiler_params=pltpu.CompilerParams(has_side_effects=EFFECT),
    )(*[in_hbm(a) for a in parts], *[in_hbm(a) for a in zones])
    return out[0], out[1], out[2:2 + ns], out[2 + ns:2 + 2 * ns], out[-1]


def _chip_exchange_wait(sections, send_sems, recv_sems, parts, zones, after, name):
    ns = len(sections)

    def body(*refs):
        p_refs, land = refs[:ns], refs[ns:2 * ns]
        for cp in _chip_copies(sections, p_refs, land, refs[2 * ns], refs[2 * ns + 1]):
            cp.wait_send()
            cp.wait_recv()

    out = pl.pallas_call(
        body, name=name,
        out_shape=tuple(pltpu.HBM(a.shape, a.dtype) for a in (*parts, *zones)),
        in_specs=[HBM] * (2 * ns) + [SEM, SEM, ANY],
        out_specs=(HBM,) * (2 * ns),
        input_output_aliases={i: i for i in range(2 * ns)},
        compiler_params=pltpu.CompilerParams(has_side_effects=EFFECT),
    )(*parts, *zones, send_sems, recv_sems, after)
    return out[:ns], out[ns:]


def _grad_finish(parts, far, chip):
    def body(chip_ref, *refs):
        p_refs, b_refs, g_refs = refs[:N_SEC], refs[N_SEC:2 * N_SEC], refs[2 * N_SEC:]
        for s in range(N_SEC):
            g = p_refs[s][0].astype(F32)
            for j in range(3):
                g = g + b_refs[s][j].astype(F32)
            g_refs[s][...] = g

    half = [(rows // 2, cols) for _, rows, cols in SECTIONS]
    return pl.pallas_call(
        body, name="grad_finish",
        out_shape=tuple(jax.ShapeDtypeStruct((rows, cols), F32) for _, rows, cols in SECTIONS),
        grid_spec=pltpu.PrefetchScalarGridSpec(
            num_scalar_prefetch=1, grid=(2,),
            in_specs=[pl.BlockSpec((1, r, c), lambda i, chip: (chip[0], i, 0)) for r, c in half]
            + [pl.BlockSpec((3, r, c), lambda i, chip: (0, i, 0)) for r, c in half],
            out_specs=tuple(pl.BlockSpec((r, c), lambda i, chip: (i, 0)) for r, c in half)),
        compiler_params=_params(dimension_semantics=("parallel",)),
    )(chip, *parts, *far)


def _sum_devices(parts, rows):
    tr = rows // 2

    def body(*refs):
        s = refs[0][...]
        for d in range(1, N_DEV):
            s = s + refs[d][...]
        refs[N_DEV][...] = s

    return pl.pallas_call(
        body, name="small_grad_sum",
        out_shape=jax.ShapeDtypeStruct((rows, LANES), F32),
        grid=(2,),
        in_specs=[pl.BlockSpec((tr, LANES), lambda i, d=d: (2 * d + i, 0)) for d in range(N_DEV)],
        out_specs=pl.BlockSpec((tr, LANES), lambda i: (i, 0)),
        compiler_params=_params(dimension_semantics=("parallel",)),
    )(*([parts] * N_DEV))


def _adamw(w, g, m, v, name):
    rows, cols = w.shape
    tr = rows
    while tr * cols * 4 > (1 << 20) and tr % 16 == 0:
        tr //= 2
    c1 = 1.0 / (1.0 - ADAM_B1 ** ADAM_STEP)
    c2 = 1.0 / (1.0 - ADAM_B2 ** ADAM_STEP)

    def body(w_ref, g_ref, m_ref, v_ref, d_ref, nm_ref, nv_ref):
        gv = g_ref[...]
        nm = ADAM_B1 * m_ref[...] + (1.0 - ADAM_B1) * gv
        nv = ADAM_B2 * v_ref[...] + (1.0 - ADAM_B2) * (gv * gv)
        nm_ref[...] = nm
        nv_ref[...] = nv
        d_ref[...] = (-ADAM_LR) * ((nm * c1) / (jnp.sqrt(nv * c2) + ADAM_EPS) + ADAM_WD * w_ref[...])

    spec = pl.BlockSpec((tr, cols), lambda i: (i, 0))
    shape = jax.ShapeDtypeStruct((rows, cols), F32)
    return pl.pallas_call(
        body, name=name,
        out_shape=(shape, shape, shape),
        grid=(rows // tr,),
        in_specs=[spec] * 4, out_specs=(spec,) * 3,
        compiler_params=_params(dimension_semantics=("parallel",)),
    )(w, g, m, v)


NAMES = ("ln1_g", "w_in", "b_in", "rpb", "w_att_o", "conv_w", "conv_b", "w_rg_a", "b_rg_a", "w_rg_i",
         "b_rg_i", "lru_lambda", "w_rec_o", "w_out", "ln2_g", "w_ff1", "w_ff2", "lnf_g")
TRANSPOSED = {"w_in": "w_in_t", "w_att_o": "w_att_o_t", "w_ff1": "w_ff1_t"}
ROW_SHARDED = ("w_rec_o", "w_out", "w_ff2")
REPLICATED = (("ln1_g", (1, D)), ("b_in", (1, D_IN)), ("rpb", (N_HEADS * N_RPB_R, N_RPB_C)),
              ("conv_b", (1, D_REC)), ("w_rg_a", (2 * N_REC_BLOCKS * REC_BLOCK, REC_BLOCK)),
              ("w_rg_i", (2 * N_REC_BLOCKS * REC_BLOCK, REC_BLOCK)), ("ln2_g", (1, D)), ("lnf_g", (1, D)))
SMALL_ROWS = 2160


def _chan_bits(vectors):
    chan = jnp.concatenate(vectors, axis=0)
    bits = lax.bitcast_convert_type(chan, BF16).reshape(-1)
    return jnp.pad(bits, (0, CHAN_BLOCK_ROWS * D - bits.shape[0])).reshape(CHAN_BLOCK_ROWS, D)


def _chan_from_bits(gathered):
    bits = gathered.reshape(N_DEV, CHAN_BLOCK_ROWS * D)[:, :2 * N_CHAN_ROWS * LANES]
    chan = lax.bitcast_convert_type(bits.reshape(N_DEV, N_CHAN_ROWS, LANES, 2), F32)
    return chan.transpose(1, 0, 2).reshape(N_CHAN_ROWS, D)


def kernel(x, ln1_g, w_in, b_in, rpb, w_att_o, conv_w, conv_b, w_rg_a, b_rg_a, w_rg_i, b_rg_i, lru_lambda, w_rec_o, w_out, ln2_g, w_ff1, w_ff2, lnf_g, loss_target, m_ln1_g, m_w_in, m_b_in, m_rpb, m_w_att_o, m_conv_w, m_conv_b, m_w_rg_a, m_b_rg_a, m_w_rg_i, m_b_rg_i, m_lru_lambda, m_w_rec_o, m_w_out, m_ln2_g, m_w_ff1, m_w_ff2, m_lnf_g, v_ln1_g, v_w_in, v_b_in, v_rpb, v_w_att_o, v_conv_w, v_conv_b, v_w_rg_a, v_b_rg_a, v_w_rg_i, v_b_rg_i, v_lru_lambda, v_w_rec_o, v_w_out, v_ln2_g, v_w_ff1, v_w_ff2, v_lnf_g):
    w = dict(zip(NAMES, (ln1_g, w_in, b_in, rpb, w_att_o, conv_w, conv_b, w_rg_a, b_rg_a, w_rg_i,
                         b_rg_i, lru_lambda, w_rec_o, w_out, ln2_g, w_ff1, w_ff2, lnf_g)))
    m = dict(zip(NAMES, (m_ln1_g, m_w_in, m_b_in, m_rpb, m_w_att_o, m_conv_w, m_conv_b, m_w_rg_a,
                         m_b_rg_a, m_w_rg_i, m_b_rg_i, m_lru_lambda, m_w_rec_o, m_w_out, m_ln2_g,
                         m_w_ff1, m_w_ff2, m_lnf_g)))
    v = dict(zip(NAMES, (v_ln1_g, v_w_in, v_b_in, v_rpb, v_w_att_o, v_conv_w, v_conv_b, v_w_rg_a,
                         v_b_rg_a, v_w_rg_i, v_b_rg_i, v_lru_lambda, v_w_rec_o, v_w_out, v_ln2_g,
                         v_w_ff1, v_w_ff2, v_lnf_g)))
    xi, yi, ci = _position()

    shard = {t: w[n][0].T.astype(BF16) for n, t in TRANSPOSED.items()}
    shard.update({n: w[n][0].astype(BF16) for n in ROW_SHARDED})
    shard["chan"] = _chan_bits([w[n][0] for n, _ in CHAN])
    gathered = dict(zip((n for n, _, _ in SECTIONS),
                        _all_gather([shard[n] for n, _, _ in SECTIONS], "weight_all_gather")))
    chan = _chan_from_bits(gathered.pop("chan"))
    p = dict(gathered)
    r0 = 0
    for n, rows in CHAN:
        p[n] = chan[r0:r0 + rows]
        r0 += rows
    p.update(ln1_g=w["ln1_g"], b_in=w["b_in"], rpb=w["rpb"][0], conv_b=w["conv_b"],
             w_rg_a=w["w_rg_a"][0], w_rg_i=w["w_rg_i"][0], ln2_g=w["ln2_g"],
             lnf_g=w["lnf_g"].reshape(1, D))

    core = jnp.reshape(ci, (1,)).astype(jnp.int32)
    chip = jnp.reshape(2 * xi + yi, (1,)).astype(jnp.int32)
    early_sections, late_sections = SECTIONS[1:], SECTIONS[:1]
    in_flight = {}

    def reduce_early(grads):
        chan_g = jnp.concatenate([grads[n] for n, _ in CHAN], axis=0)
        chan_g = chan_g.reshape(N_CHAN_ROWS, N_DEV, LANES).transpose(1, 0, 2).astype(BF16)
        chan_g = jnp.pad(chan_g.reshape(N_DEV, -1), ((0, 0), (0, CHAN_BLOCK_ROWS * D - N_CHAN_ROWS * LANES)))
        grads["chan"] = chan_g.reshape(N_DEV * CHAN_BLOCK_ROWS, D)
        sect = [grads[n] for n, _, _ in early_sections]
        got = _pair_exchange(early_sections, sect, "grad_pair_exchange_early")
        parts = _pair_add(early_sections, sect, got, core, "grad_pair_add_early")
        in_flight["early"] = _chip_exchange_start(early_sections, parts, "grad_chip_exchange_start")
        return in_flight["early"][-1][0, 0]

    loss_part, grad_x, grads = _local_step(x[0], loss_target[0], p, reduce_early)
    sect = [grads[n] for n, _, _ in late_sections]
    got = _pair_exchange(late_sections, sect, "grad_pair_exchange_late")
    late_parts = _pair_add(late_sections, sect, got, core, "grad_pair_add_late")
    late_far = _chip_exchange(late_sections, late_parts, "grad_chip_exchange_late")
    send_sems, recv_sems, early_parts, zones, _ = in_flight["early"]
    early_parts, early_far = _chip_exchange_wait(early_sections, send_sems, recv_sems, early_parts, zones,
                                                 late_far[0], "grad_chip_exchange_wait")
    summed = dict(zip((n for n, _, _ in SECTIONS),
                      _grad_finish([*late_parts, *early_parts], [*late_far, *early_far], chip)))


    flat = jnp.concatenate([grads[n].reshape(-1) for n, _ in REPLICATED] + [loss_part.reshape(-1)])
    n_small = flat.shape[0]
    flat = jnp.pad(flat, (0, SMALL_ROWS * LANES - n_small)).reshape(SMALL_ROWS, LANES)
    (small_parts,) = _all_gather([flat], "small_grad_all_gather")
    small = _sum_devices(small_parts, SMALL_ROWS).reshape(-1)
    loss = small[n_small - 1]

    g, delta, new_m, new_v = {}, {}, {}, {}

    def update(n, g2, shape2):
        d2, m2, v2 = _adamw(w[n].reshape(shape2), g2, m[n].reshape(shape2), v[n].reshape(shape2),
                            "adamw_" + n)
        g[n], delta[n], new_m[n], new_v[n] = (a.reshape(w[n].shape) for a in (g2, d2, m2, v2))

    for n in ROW_SHARDED:
        update(n, summed[n], summed[n].shape)
    for n, t in TRANSPOSED.items():
        update(n, summed[t].T, summed[t].shape[::-1])
    chan_back = summed["chan"].reshape(-1)[:N_CHAN_ROWS * LANES].reshape(N_CHAN_ROWS, LANES)
    r0 = 0
    for n, rows in CHAN:
        update(n, chan_back[r0:r0 + rows], (rows, LANES))
        r0 += rows
    o = 0
    for n, shape2 in REPLICATED:
        size = shape2[0] * shape2[1]
        update(n, small[o:o + size].reshape(shape2), shape2)
        o += size

    return (loss, grad_x[None], *[g[n] for n in NAMES], *[delta[n] for n in NAMES],
            *[new_m[n] for n in NAMES], *[new_v[n] for n in NAMES])
```

```python
import math

import numpy as np
import jax
import jax.numpy as jnp
from jax import lax
from jax.experimental import pallas as pl
from jax.experimental.pallas import tpu as pltpu

F32 = jnp.float32
BF16 = jnp.bfloat16

T = 2048
D = 1024
D_ATT = 512
D_REC = 1024
D_FF = 4096
D_IN = 5632
N_HEADS = 8
DH = 64
GRID_W = 64
ROWS = T // GRID_W
WIN_H = 8
WIN_W = 16
KWIN = WIN_H * GRID_W
N_RPB_R = 2 * WIN_H - 1
N_RPB_C = 2 * WIN_W - 1
N_REC_BLOCKS = 16
REC_BLOCK = 64
CG = 128
N_CG = D_REC // CG
LRU_C = 8.0
EPS = 1e-6
N_DEV = 8
N_CHIPS = 4
LANES = 128

ADAM_LR = 0.001
ADAM_B1 = 0.9
ADAM_B2 = 0.999
ADAM_EPS = 1e-08
ADAM_WD = 0.01
ADAM_STEP = 10

MESH_AXES = ("x", "y", "c")
VMEM_LIMIT = 56 * 1024 * 1024

TILE = 512
DZ_SEGMENTS = ((0, 1), (1, 1), (2, 1), (3, 2), (5, 2), (7, 2), (9, 2))
N_DZ_TILES = D_IN // TILE


def _params(**kw):
    return pltpu.CompilerParams(vmem_limit_bytes=VMEM_LIMIT, **kw)


def _att_tables():
    rq = np.arange(2 * GRID_W) % GRID_W
    kc = np.arange(KWIN) % GRID_W
    win_start = np.clip(rq - WIN_W // 2, 0, GRID_W - WIN_W)
    valid = (kc[None, :] >= win_start[:, None]) & (kc[None, :] < win_start[:, None] + WIN_W)
    return valid.astype(np.float32), _pair_mask()


def _pair_mask():
    half = np.arange(2 * DH) // DH
    return (half[:, None] == half[None, :]).astype(np.float32)


def _dup_table():
    return np.concatenate([np.eye(REC_BLOCK, dtype=np.float32)] * 2, axis=1)


def _toeplitz_table():
    q = np.arange(GRID_W)[:, None]
    kc = np.arange(GRID_W)[None, :]
    dc = (kc - q + WIN_W - 1).reshape(-1)
    e = np.zeros((128, GRID_W * GRID_W), np.float32)
    ok = (dc >= 0) & (dc < N_RPB_C)
    e[dc[ok], np.arange(GRID_W * GRID_W)[ok]] = 1.0
    return e


def _row_shift_table():
    s = np.zeros((16, WIN_H * WIN_H), np.float32)
    for oi in range(WIN_H):
        for i in range(WIN_H):
            s[i - oi + WIN_H - 1, oi * WIN_H + i] = 1.0
    return s


def _sigmoid(x):
    return 0.5 * jnp.tanh(0.5 * x) + 0.5


def _softplus(x):
    return jnp.maximum(x, 0.0) + jnp.log(1.0 + jnp.exp(-jnp.abs(x)))


def _one_minus_square(log_a, a):
    x = 2.0 * log_a
    series = -x * (1.0 + x * (0.5 + x * (1.0 / 6.0)))
    return jnp.where(x > -0.02, series, 1.0 - a * a)


_GELU_C = math.sqrt(2.0 / math.pi)


def _gelu_and_grad(x):
    x2 = x * x
    inner = _GELU_C * (x + 0.044715 * x * x2)
    t = jnp.tanh(inner)
    g = 0.5 * x * (1.0 + t)
    dg = 0.5 * (1.0 + t) + 0.5 * x * (1.0 - t * t) * _GELU_C * (1.0 + 3.0 * 0.044715 * x2)
    return g, dg


def _dot(a, b):
    return jnp.dot(a, b, preferred_element_type=F32)


def _dot_nt(a, b):
    return lax.dot_general(a, b, (((1,), (1,)), ((), ())), preferred_element_type=F32)


def _dot_tn(a, b):
    return lax.dot_general(a, b, (((0,), (0,)), ((), ())), preferred_element_type=F32)


def _dot_exact(a, b):
    return jnp.dot(a, b, precision=lax.Precision.HIGHEST, preferred_element_type=F32)


def _shift_rows(x, s):
    n = x.shape[0]
    rows = lax.broadcasted_iota(jnp.int32, x.shape, 0)
    y = pltpu.roll(x, s % n, 0)
    if s > 0:
        return jnp.where(rows >= s, y, 0.0)
    return jnp.where(rows < n + s, y, 0.0)


def _rms_bwd(dh, xh, r, g):
    dxh = dh * g
    return r * (dxh - xh * jnp.mean(dxh * xh, axis=-1, keepdims=True))


def _matmul(a, b, mode, out_dtype, name, tm=512, tn=1024, tk=2048):
    if mode == "nn":
        (m, k), (k2, n) = a.shape, b.shape
    elif mode == "nt":
        (m, k), (n, k2) = a.shape, b.shape
    else:
        (k, m), (k2, n) = a.shape, b.shape
    assert k == k2
    tm, tn, tk = min(tm, m), min(tn, n), min(tk, k)
    assert m % tm == 0 and n % tn == 0 and k % tk == 0
    nk = k // tk
    dot = {"nn": _dot, "nt": _dot_nt, "tn": _dot_tn}[mode]

    def body(a_ref, b_ref, o_ref, acc):
        kk = pl.program_id(2)
        part = dot(a_ref[...].astype(BF16), b_ref[...].astype(BF16))
        if nk == 1:
            o_ref[...] = part.astype(out_dtype)
            return

        @pl.when(kk == 0)
        def _():
            acc[...] = part

        @pl.when(kk > 0)
        def _():
            acc[...] += part

        @pl.when(kk == nk - 1)
        def _():
            o_ref[...] = acc[...].astype(out_dtype)

    if mode == "tn":
        a_spec = pl.BlockSpec((tk, tm), lambda i, j, kk: (kk, i))
    else:
        a_spec = pl.BlockSpec((tm, tk), lambda i, j, kk: (i, kk))
    if mode == "nt":
        b_spec = pl.BlockSpec((tn, tk), lambda i, j, kk: (j, kk))
    else:
        b_spec = pl.BlockSpec((tk, tn), lambda i, j, kk: (kk, j))
    return pl.pallas_call(
        body, name=name,
        out_shape=jax.ShapeDtypeStruct((m, n), out_dtype),
        grid=(m // tm, n // tn, nk),
        in_specs=[a_spec, b_spec],
        out_specs=pl.BlockSpec((tm, tn), lambda i, j, kk: (i, j)),
        scratch_shapes=[pltpu.VMEM((tm, tn) if nk > 1 else (8, LANES), F32)],
        compiler_params=_params(dimension_semantics=("parallel", "parallel", "arbitrary")),
    )(a, b)


def _in_proj(x, g1, w_in_t, b_in):
    tm = 512

    def body(x_ref, g_ref, w_ref, b_ref, qkv_ref, uy_ref, gg_ref, h_ref, h_scr):
        j = pl.program_id(1)

        @pl.when(j == 0)
        def _():
            xv = x_ref[...]
            r = lax.rsqrt(jnp.mean(xv * xv, axis=-1, keepdims=True) + EPS)
            h = ((xv * r) * g_ref[...]).astype(BF16)
            h_scr[...] = h
            h_ref[...] = h

        z = _dot_nt(h_scr[...], w_ref[...]) + b_ref[...]

        @pl.when(j < 3)
        def _():
            qkv_ref[...] = z.astype(BF16)

        @pl.when((j >= 3) & (j < 7))
        def _():
            uy_ref[...] = z

        @pl.when(j >= 7)
        def _():
            gg_ref[...] = z

    return pl.pallas_call(
        body, name="in_proj",
        out_shape=(jax.ShapeDtypeStruct((T, 3 * D_ATT), BF16),
                   jax.ShapeDtypeStruct((T, 2 * D_REC), F32),
                   jax.ShapeDtypeStruct((T, 2 * D), F32),
                   jax.ShapeDtypeStruct((T, D), BF16)),
        grid=(T // tm, N_DZ_TILES),
        in_specs=[pl.BlockSpec((tm, D), lambda i, j: (i, 0)),
                  pl.BlockSpec((1, D), lambda i, j: (0, 0)),
                  pl.BlockSpec((TILE, D), lambda i, j: (j, 0)),
                  pl.BlockSpec((1, TILE), lambda i, j: (0, j))],
        out_specs=(pl.BlockSpec((tm, TILE), lambda i, j: (i, jnp.minimum(j, 2))),
                   pl.BlockSpec((tm, TILE), lambda i, j: (i, jnp.clip(j - 3, 0, 3))),
                   pl.BlockSpec((tm, TILE), lambda i, j: (i, jnp.clip(j - 7, 0, 3))),
                   pl.BlockSpec((tm, D), lambda i, j: (i, 0))),
        scratch_shapes=[pltpu.VMEM((tm, D), BF16)],
        compiler_params=_params(dimension_semantics=("parallel", "arbitrary")),
    )(x, g1, w_in_t, b_in)


def _segment_spec(rows, seg, row_index):
    off, n = seg
    if row_index:
        return pl.BlockSpec((rows, TILE), lambda i, j: (i, jnp.clip(j - off, 0, n - 1)))
    return pl.BlockSpec((rows, TILE), lambda j, kk: (kk, jnp.clip(j - off, 0, n - 1)))


def _dh_norm1_bwd(dz_segments, w_in_t, x, g1, dx1):
    tm = 512

    def body(*refs):
        seg_refs = refs[:7]
        w_ref, x_ref, g_ref, dx1_ref, gx_ref, dg_ref, acc = refs[7:]
        i, kk = pl.program_id(0), pl.program_id(1)

        @pl.when(kk == 0)
        def _():
            acc[...] = jnp.zeros_like(acc)

        for s, (off, n) in enumerate(DZ_SEGMENTS):
            @pl.when((kk >= off) & (kk < off + n))
            def _(s=s):
                acc[...] += _dot(seg_refs[s][...], w_ref[...])

        @pl.when((i == 0) & (kk == 0))
        def _():
            dg_ref[...] = jnp.zeros_like(dg_ref)

        @pl.when(kk == N_DZ_TILES - 1)
        def _():
            xv = x_ref[...]
            r = lax.rsqrt(jnp.mean(xv * xv, axis=-1, keepdims=True) + EPS)
            xh = xv * r
            dh = acc[...]
            dg_ref[...] += jnp.sum(dh * xh, axis=0, keepdims=True)
            gx_ref[...] = dx1_ref[...] + _rms_bwd(dh, xh, r, g_ref[...])

    tok = pl.BlockSpec((tm, D), lambda i, j: (i, 0))
    vec = pl.BlockSpec((1, D), lambda i, j: (0, 0))
    return pl.pallas_call(
        body, name="dh_norm1_bwd",
        out_shape=(jax.ShapeDtypeStruct((T, D), F32), jax.ShapeDtypeStruct((1, D), F32)),
        grid=(T // tm, N_DZ_TILES),
        in_specs=[_segment_spec(tm, seg, True) for seg in DZ_SEGMENTS]
        + [pl.BlockSpec((TILE, D), lambda i, j: (j, 0)), tok, vec, tok],
        out_specs=(tok, vec),
        scratch_shapes=[pltpu.VMEM((tm, D), F32)],
        compiler_params=_params(dimension_semantics=("arbitrary", "arbitrary")),
    )(*dz_segments, w_in_t, x, g1, dx1)


def _grad_w_in(dz_segments, h):
    tk = 1024
    nk = T // tk

    def body(*refs):
        seg_refs = refs[:7]
        h_ref, gw_ref, gb_ref, acc = refs[7:]
        j, kk = pl.program_id(0), pl.program_id(1)

        @pl.when(kk == 0)
        def _():
            acc[...] = jnp.zeros_like(acc)
            gb_ref[...] = jnp.zeros_like(gb_ref)

        for s, (off, n) in enumerate(DZ_SEGMENTS):
            @pl.when((j >= off) & (j < off + n))
            def _(s=s):
                a = seg_refs[s][...]
                acc[...] += _dot_tn(a, h_ref[...])
                gb_ref[...] += jnp.sum(a.astype(F32), axis=0, keepdims=True)

        @pl.when(kk == nk - 1)
        def _():
            gw_ref[...] = acc[...].astype(BF16)

    return pl.pallas_call(
        body, name="grad_w_in",
        out_shape=(jax.ShapeDtypeStruct((D_IN, D), BF16), jax.ShapeDtypeStruct((1, D_IN), F32)),
        grid=(N_DZ_TILES, nk),
        in_specs=[_segment_spec(tk, seg, False) for seg in DZ_SEGMENTS]
        + [pl.BlockSpec((tk, D), lambda j, kk: (kk, 0))],
        out_specs=(pl.BlockSpec((TILE, D), lambda j, kk: (j, 0)),
                   pl.BlockSpec((1, TILE), lambda j, kk: (0, j))),
        scratch_shapes=[pltpu.VMEM((TILE, D), F32)],
        compiler_params=_params(dimension_semantics=("parallel", "arbitrary")),
    )(*dz_segments, h)


def _rpb_expand(rpb):
    rpb2 = jnp.pad(rpb.reshape(N_HEADS * N_RPB_R, N_RPB_C), ((0, 0), (0, 128 - N_RPB_C)))
    table = jnp.asarray(_toeplitz_table())

    def body(r_ref, e_ref, o_ref):
        o_ref[...] = _dot_exact(r_ref[...], e_ref[...])

    tb = pl.pallas_call(
        body, name="rpb_expand",
        out_shape=jax.ShapeDtypeStruct((N_HEADS * N_RPB_R, GRID_W * GRID_W), F32),
    )(rpb2, table)
    tb = tb.reshape(N_HEADS, N_RPB_R, GRID_W, GRID_W)
    variants = []
    for oi in range(WIN_H):
        sl = tb[:, WIN_H - 1 - oi: 2 * WIN_H - 1 - oi]
        sl = sl.transpose(0, 2, 1, 3).reshape(N_HEADS // 2, 2 * GRID_W, KWIN)
        variants.append(sl)
    return jnp.stack(variants, axis=0)


def _rpb_reduce(gbias):
    g = gbias.reshape(N_HEADS // 2, WIN_H, 2, GRID_W, WIN_H, GRID_W)
    g = g.transpose(0, 2, 1, 4, 3, 5).reshape(N_HEADS, WIN_H * WIN_H, GRID_W * GRID_W)
    table = jnp.asarray(_toeplitz_table().T.copy())
    shift = jnp.asarray(_row_shift_table())

    def body(g_ref, e_ref, s_ref, o_ref):
        r = _dot_exact(g_ref[0], e_ref[...])
        o_ref[0] = _dot_exact(s_ref[...], r)

    out = pl.pallas_call(
        body, name="rpb_reduce",
        out_shape=jax.ShapeDtypeStruct((N_HEADS, 16, 128), F32),
        grid=(N_HEADS,),
        in_specs=[pl.BlockSpec((1, WIN_H * WIN_H, GRID_W * GRID_W), lambda h: (h, 0, 0)),
                  pl.BlockSpec((GRID_W * GRID_W, 128), lambda h: (0, 0)),
                  pl.BlockSpec((16, WIN_H * WIN_H), lambda h: (0, 0))],
        out_specs=pl.BlockSpec((1, 16, 128), lambda h: (h, 0, 0)),
        compiler_params=_params(dimension_semantics=("arbitrary",)),
    )(g, table, shift)
    return out[:, :N_RPB_R, :N_RPB_C]


def _att_scores(q_ref, k_ref, bias_ref, valid, hmask, r):
    rs = jnp.clip(r - WIN_H // 2, 0, ROWS - WIN_H)
    oi = r - rs
    q0 = pl.multiple_of(r * GRID_W, GRID_W)
    k0 = pl.multiple_of(rs * GRID_W, GRID_W)
    q_r = q_ref[pl.ds(q0, GRID_W), :]
    q2 = jnp.where(hmask, jnp.concatenate([q_r, q_r], axis=0), jnp.zeros((), BF16))
    kw = k_ref[pl.ds(k0, KWIN), :]
    s = _dot_nt(q2, kw) * (DH ** -0.5) + bias_ref[oi, 0]
    s = jnp.where(valid, s, -1e30)
    m = jnp.max(s, axis=-1, keepdims=True)
    p = jnp.exp(s - m)
    p = p / jnp.sum(p, axis=-1, keepdims=True)
    return p, q2, kw, q0, k0, oi


def _att_fwd(qkv, bias):
    valid_np, hmask_np = _att_tables()

    def body(q_ref, k_ref, v_ref, bias_ref, valid_ref, hmask_ref, o_ref):
        valid = valid_ref[...] > 0.5
        hmask = hmask_ref[...] > 0.5
        first_head = lax.broadcasted_iota(jnp.int32, (GRID_W, 2 * DH), 1) < DH

        def row(r, carry):
            p, _, _, q0, k0, _ = _att_scores(q_ref, k_ref, bias_ref, valid, hmask, r)
            o2 = _dot(p.astype(BF16), v_ref[pl.ds(k0, KWIN), :])
            o_ref[pl.ds(q0, GRID_W), :] = jnp.where(first_head, o2[:GRID_W], o2[GRID_W:]).astype(BF16)
            return carry

        lax.fori_loop(0, ROWS, row, 0, unroll=4)

    col = lambda off: pl.BlockSpec((T, 2 * DH), lambda hp: (0, hp + off))
    return pl.pallas_call(
        body, name="att_fwd",
        out_shape=jax.ShapeDtypeStruct((T, D_ATT), BF16),
        grid=(N_HEADS // 2,),
        in_specs=[col(0), col(4), col(8),
                  pl.BlockSpec((WIN_H, 1, 2 * GRID_W, KWIN), lambda hp: (0, hp, 0, 0)),
                  pl.BlockSpec((2 * GRID_W, KWIN), lambda hp: (0, 0)),
                  pl.BlockSpec((2 * DH, 2 * DH), lambda hp: (0, 0))],
        out_specs=pl.BlockSpec((T, 2 * DH), lambda hp: (0, hp)),
        compiler_params=_params(dimension_semantics=("parallel",)),
    )(qkv, qkv, qkv, bias, jnp.asarray(valid_np), jnp.asarray(hmask_np))


def _att_bwd(qkv, bias, datt, after):
    valid_np, hmask_np = _att_tables()

    def body(q_ref, k_ref, v_ref, do_ref, bias_ref, valid_ref, hmask_ref,
             dq_ref, dk_ref, dv_ref, gb_ref, dk_acc, dv_acc):
        valid = valid_ref[...] > 0.5
        hmask = hmask_ref[...] > 0.5
        first_head = lax.broadcasted_iota(jnp.int32, (GRID_W, 2 * DH), 1) < DH
        dk_acc[...] = jnp.zeros_like(dk_acc)
        dv_acc[...] = jnp.zeros_like(dv_acc)
        gb_ref[...] = jnp.zeros_like(gb_ref)

        def row(r, carry):
            p, q2, kw, q0, k0, oi = _att_scores(q_ref, k_ref, bias_ref, valid, hmask, r)
            do_r = do_ref[pl.ds(q0, GRID_W), :]
            do2 = jnp.where(hmask, jnp.concatenate([do_r, do_r], axis=0), jnp.zeros((), BF16))
            vw = v_ref[pl.ds(k0, KWIN), :]
            dp = _dot_nt(do2, vw)
            ds = p * (dp - jnp.sum(dp * p, axis=-1, keepdims=True))
            p16 = p.astype(BF16)
            ds16 = ds.astype(BF16)
            dv_acc[pl.ds(k0, KWIN), :] += _dot_tn(p16, do2)
            dk_acc[pl.ds(k0, KWIN), :] += _dot_tn(ds16, q2) * (DH ** -0.5)
            dq2 = _dot(ds16, kw) * (DH ** -0.5)
            dq_ref[pl.ds(q0, GRID_W), :] = jnp.where(first_head, dq2[:GRID_W], dq2[GRID_W:]).astype(BF16)
            gb_ref[0, oi] += ds
            return carry

        lax.fori_loop(0, ROWS, row, 0, unroll=4)
        dk_ref[...] = dk_acc[...].astype(BF16)
        dv_ref[...] = dv_acc[...].astype(BF16)

    col = lambda off: pl.BlockSpec((T, 2 * DH), lambda hp: (0, hp + off))
    out_col = pl.BlockSpec((T, 2 * DH), lambda hp: (0, hp))
    return pl.pallas_call(
        body, name="att_bwd",
        out_shape=(jax.ShapeDtypeStruct((T, D_ATT), BF16),) * 3
        + (jax.ShapeDtypeStruct((N_HEADS // 2, WIN_H, 2 * GRID_W, KWIN), F32),),
        grid=(N_HEADS // 2,),
        in_specs=[col(0), col(4), col(8), col(0),
                  pl.BlockSpec((WIN_H, 1, 2 * GRID_W, KWIN), lambda hp: (0, hp, 0, 0)),
                  pl.BlockSpec((2 * GRID_W, KWIN), lambda hp: (0, 0)),
                  pl.BlockSpec((2 * DH, 2 * DH), lambda hp: (0, 0))],
        out_specs=(out_col, out_col, out_col,
                   pl.BlockSpec((1, WIN_H, 2 * GRID_W, KWIN), lambda hp: (hp, 0, 0, 0))),
        scratch_shapes=[pltpu.VMEM((T, 2 * DH), F32), pltpu.VMEM((T, 2 * DH), F32)],
        compiler_params=_params(dimension_semantics=("parallel",)),
    )(qkv, qkv, qkv, datt, bias, jnp.asarray(valid_np) + after, jnp.asarray(hmask_np))


def _conv_taps(up):
    return (_shift_rows(up, 2), _shift_rows(up, 1), up, _shift_rows(up, -1))


def _pair_block_diag(w_pair, dup, same_half):
    return jnp.where(same_half, _dot(w_pair.astype(BF16), dup), 0.0).astype(BF16)


def _gates(u, u16, wa, ba, wi, bi, lam):
    r = _sigmoid(_dot(u16, wa) + ba)
    ig = _sigmoid(_dot(u16, wi) + bi)
    sp = _softplus(-lam)
    log_a = (-LRU_C) * r * sp
    a = jnp.exp(log_a)
    mult = jnp.sqrt(jnp.maximum(_one_minus_square(log_a, a), 0.0))
    return r, ig, sp, a, mult


SCAN_BLOCKS = 2


def _scans(jobs):
    c = jobs[0][0].shape[1]
    nblk = T // 8
    rows = lax.broadcasted_iota(jnp.int32, (8, c), 0)

    def block(a, b, reverse):
        for s in (1, 2, 4):
            if reverse:
                keep = rows < 8 - s
                a_s = jnp.where(keep, pltpu.roll(a, 8 - s, 0), 1.0)
                b_s = jnp.where(keep, pltpu.roll(b, 8 - s, 0), 0.0)
            else:
                keep = rows >= s
                a_s = jnp.where(keep, pltpu.roll(a, s, 0), 1.0)
                b_s = jnp.where(keep, pltpu.roll(b, s, 0), 0.0)
            b = a * b_s + b
            a = a * a_s
        return a, b

    def step(i, carry):
        out = []
        for (a_ref, b_ref, h_ref, reverse), h_prev in zip(jobs, carry):
            for u in range(SCAN_BLOCKS):
                blk = i * SCAN_BLOCKS + u
                if reverse:
                    blk = nblk - 1 - blk
                t0 = pl.multiple_of(blk * 8, 8)
                a, b = block(a_ref[pl.ds(t0, 8), :], b_ref[pl.ds(t0, 8), :], reverse)
                h = a * h_prev + b
                h_ref[pl.ds(t0, 8), :] = h
                h_prev = jnp.broadcast_to(h[0:1] if reverse else h[7:8], (8, c))
            out.append(h_prev)
        return tuple(out)

    lax.fori_loop(0, nblk // SCAN_BLOCKS, step, tuple(jnp.zeros((8, c), F32) for _ in jobs))


def _rec_specs():
    tok = lambda off: pl.BlockSpec((T, CG), lambda g: (0, g + off))
    per_ch = lambda rows: pl.BlockSpec((rows, CG), lambda g: (0, g))
    wspec = pl.BlockSpec((2, 1, CG, REC_BLOCK), lambda g: (0, g, 0, 0))
    const = lambda shape: pl.BlockSpec(shape, lambda g: (0, 0))
    return tok, per_ch, wspec, const


def _rec_fwd(uy, conv_w, conv_b, w_a, b_a, w_i, b_i, lam):
    tok, per_ch, wspec, const = _rec_specs()

    def body(up_ref, yb_ref, cw_ref, cb_ref, wa_ref, ba_ref, wi_ref, bi_ref, lam_ref, dup_ref, half_ref,
             hf_ref, hb_ref, yrec_ref, a_f, bx_f, a_b, bx_b):
        dup = dup_ref[...]
        same_half = half_ref[...] > 0.5
        taps = _conv_taps(up_ref[...])
        u = cb_ref[...]
        for j in range(4):
            u = u + taps[j] * cw_ref[j:j + 1, :]
        u16 = u.astype(BF16)
        for d, (a_s, bx_s) in enumerate(((a_f, bx_f), (a_b, bx_b))):
            wa = _pair_block_diag(wa_ref[d, 0], dup, same_half)
            wi = _pair_block_diag(wi_ref[d, 0], dup, same_half)
            _, ig, _, a, mult = _gates(u, u16, wa, ba_ref[d:d + 1, :], wi, bi_ref[d:d + 1, :],
                                       lam_ref[d:d + 1, :])
            a_s[...] = a
            bx_s[...] = mult * (ig * u)
        _scans([(a_f, bx_f, hf_ref, False), (a_b, bx_b, hb_ref, True)])
        gelu, _ = _gelu_and_grad(yb_ref[...])
        yrec_ref[...] = ((hf_ref[...] + hb_ref[...]) * gelu).astype(BF16)

    return pl.pallas_call(
        body, name="rec_fwd",
        out_shape=(jax.ShapeDtypeStruct((T, D_REC), F32), jax.ShapeDtypeStruct((T, D_REC), F32),
                   jax.ShapeDtypeStruct((T, D_REC), BF16)),
        grid=(N_CG,),
        in_specs=[tok(0), tok(N_CG), per_ch(4), per_ch(1), wspec, per_ch(2), wspec, per_ch(2), per_ch(2),
                  const((REC_BLOCK, CG)), const((CG, CG))],
        out_specs=(tok(0), tok(0), tok(0)),
        scratch_shapes=[pltpu.VMEM((T, CG), F32)] * 4,
        compiler_params=_params(dimension_semantics=("parallel",)),
    )(uy, uy, conv_w, conv_b, w_a, b_a, w_i, b_i, lam,
      jnp.asarray(_dup_table(), BF16), jnp.asarray(_pair_mask()))


def _rec_bwd(uy, hf, hb, dyrec, conv_w, conv_b, w_a, b_a, w_i, b_i, lam):
    tok, per_ch, wspec, const = _rec_specs()

    def body(up_ref, yb_ref, hf_ref, hb_ref, dy_ref, cw_ref, cb_ref, wa_ref, ba_ref, wi_ref, bi_ref,
             lam_ref, dup_ref, dupt_ref, half_ref,
             dup_out, dyb_ref, dcw_ref, dcb_ref, dwa_ref, dba_ref, dwi_ref, dbi_ref, dlam_ref,
             a_s0, a_s1, dh_s, g_s0, g_s1):
        dup = dup_ref[...]
        dup_t = dupt_ref[...]
        same_half = half_ref[...] > 0.5
        taps = _conv_taps(up_ref[...])
        u = cb_ref[...]
        for j in range(4):
            u = u + taps[j] * cw_ref[j:j + 1, :]
        u16 = u.astype(BF16)
        gelu, dgelu = _gelu_and_grad(yb_ref[...])
        dy = dy_ref[...]
        dyb_ref[...] = (dy * (hf_ref[...] + hb_ref[...]) * dgelu).astype(BF16)
        dh_s[...] = dy * gelu
        gate_values = []
        for d, a_s in enumerate((a_s0, a_s1)):
            wa = _pair_block_diag(wa_ref[d, 0], dup, same_half)
            wi = _pair_block_diag(wi_ref[d, 0], dup, same_half)
            lam_d = lam_ref[d:d + 1, :]
            r, ig, sp, a, mult = _gates(u, u16, wa, ba_ref[d:d + 1, :], wi, bi_ref[d:d + 1, :], lam_d)
            a_s[...] = _shift_rows(a, 1 if d == 1 else -1)
            gate_values.append((wa, wi, lam_d, r, ig, sp, a, mult))
        _scans([(a_s0, dh_s, g_s0, True), (a_s1, dh_s, g_s1, False)])
        du = jnp.zeros((T, CG), F32)
        for d, g_s in enumerate((g_s0, g_s1)):
            reverse = d == 1
            wa, wi, lam_d, r, ig, sp, a, mult = gate_values[d]
            g = g_s[...]
            h_prev = _shift_rows(hb_ref[...], -1) if reverse else _shift_rows(hf_ref[...], 1)
            da = g * h_prev
            dmult = g * (ig * u)
            dig = g * mult * u
            du = du + g * mult * ig
            dmult_dlog = jnp.where(mult > 0.0, -(a * a) / mult, 0.0)
            dlog_a = da * a + dmult * dmult_dlog
            dr = dlog_a * ((-LRU_C) * sp)
            dsp = jnp.sum(dlog_a * ((-LRU_C) * r), axis=0, keepdims=True)
            dlam_ref[d:d + 1, :] = dsp * (-_sigmoid(-lam_d))
            dga = dr * r * (1.0 - r)
            dgi = dig * ig * (1.0 - ig)
            dga16 = dga.astype(BF16)
            dgi16 = dgi.astype(BF16)
            du = du + _dot_nt(dga16, wa) + _dot_nt(dgi16, wi)
            dwa_ref[d, 0] = _dot_exact(jnp.where(same_half, _dot_tn(u16, dga16), 0.0), dup_t)
            dwi_ref[d, 0] = _dot_exact(jnp.where(same_half, _dot_tn(u16, dgi16), 0.0), dup_t)
            dba_ref[d:d + 1, :] = jnp.sum(dga, axis=0, keepdims=True)
            dbi_ref[d:d + 1, :] = jnp.sum(dgi, axis=0, keepdims=True)
        dcb_ref[...] = jnp.sum(du, axis=0, keepdims=True)
        for j in range(4):
            dcw_ref[j:j + 1, :] = jnp.sum(du * taps[j], axis=0, keepdims=True)
        dup_in = (_shift_rows(du, -2) * cw_ref[0:1, :] + _shift_rows(du, -1) * cw_ref[1:2, :]
                  + du * cw_ref[2:3, :] + _shift_rows(du, 1) * cw_ref[3:4, :])
        dup_out[...] = dup_in.astype(BF16)

    wshape = jax.ShapeDtypeStruct((2, N_CG, CG, REC_BLOCK), F32)
    vec = lambda rows: jax.ShapeDtypeStruct((rows, D_REC), F32)
    dup_np = _dup_table()
    return pl.pallas_call(
        body, name="rec_bwd",
        out_shape=(jax.ShapeDtypeStruct((T, D_REC), BF16), jax.ShapeDtypeStruct((T, D_REC), BF16),
                   vec(4), vec(1), wshape, vec(2), wshape, vec(2), vec(2)),
        grid=(N_CG,),
        in_specs=[tok(0), tok(N_CG), tok(0), tok(0), tok(0),
                  per_ch(4), per_ch(1), wspec, per_ch(2), wspec, per_ch(2), per_ch(2),
                  const((REC_BLOCK, CG)), const((CG, REC_BLOCK)), const((CG, CG))],
        out_specs=(tok(0), tok(0), per_ch(4), per_ch(1), wspec, per_ch(2), wspec, per_ch(2), per_ch(2)),
        scratch_shapes=[pltpu.VMEM((T, CG), F32)] * 5,
        compiler_params=_params(dimension_semantics=("parallel",)),
    )(uy, uy, hf, hb, dyrec, conv_w, conv_b, w_a, b_a, w_i, b_i, lam,
      jnp.asarray(dup_np, BF16), jnp.asarray(dup_np.T.copy()), jnp.asarray(_pair_mask()))


TM_MIX = 256


def _mix_specs():
    tok = lambda width, blk=0: pl.BlockSpec((TM_MIX, width), lambda i: (i, blk))
    full = lambda shape: pl.BlockSpec(shape, lambda i: (0, 0))
    return tok, full


def _mix_fwd(x, att, yrec, gg, w_att_o_t, w_rec_o, w_out):
    tok, full = _mix_specs()

    def body(x_ref, att_ref, yr_ref, ga_ref, gr_ref, wao_ref, wro_ref, wo_ref, x1_ref, mixed_ref):
        y_att = _dot_nt(att_ref[...], wao_ref[...])
        y_rec = _dot(yr_ref[...], wro_ref[...])
        mixed = (_sigmoid(ga_ref[...]) * y_att + _sigmoid(gr_ref[...]) * y_rec).astype(BF16)
        mixed_ref[...] = mixed
        x1_ref[...] = x_ref[...] + _dot(mixed, wo_ref[...])

    return pl.pallas_call(
        body, name="mix_fwd",
        out_shape=(jax.ShapeDtypeStruct((T, D), F32), jax.ShapeDtypeStruct((T, D), BF16)),
        grid=(T // TM_MIX,),
        in_specs=[tok(D), tok(D_ATT), tok(D_REC), tok(D, 0), tok(D, 1),
                  full((D, D_ATT)), full((D_REC, D)), full((D, D))],
        out_specs=(tok(D), tok(D)),
        compiler_params=_params(dimension_semantics=("parallel",)),
    )(x, att, yrec, gg, gg, w_att_o_t, w_rec_o, w_out)


def _mix_bwd(dx1, att, yrec, gg, w_att_o_t, w_rec_o, w_out):
    tok, full = _mix_specs()

    def body(dx_ref, att_ref, yr_ref, ga_ref, gr_ref, wao_ref, wro_ref, wo_ref,
             dga_ref, dgr_ref, dya_ref, dyr_ref, datt_ref, dyrp_ref):
        dmixed = _dot_nt(dx_ref[...].astype(BF16), wo_ref[...])
        y_att = _dot_nt(att_ref[...], wao_ref[...])
        y_rec = _dot(yr_ref[...], wro_ref[...])
        sa = _sigmoid(ga_ref[...])
        sr = _sigmoid(gr_ref[...])
        dga_ref[...] = (dmixed * y_att * sa * (1.0 - sa)).astype(BF16)
        dgr_ref[...] = (dmixed * y_rec * sr * (1.0 - sr)).astype(BF16)
        dya = (dmixed * sa).astype(BF16)
        dyr = (dmixed * sr).astype(BF16)
        dya_ref[...] = dya
        dyr_ref[...] = dyr
        datt_ref[...] = _dot(dya, wao_ref[...]).astype(BF16)
        dyrp_ref[...] = _dot_nt(dyr, wro_ref[...])

    return pl.pallas_call(
        body, name="mix_bwd",
        out_shape=(jax.ShapeDtypeStruct((T, D), BF16), jax.ShapeDtypeStruct((T, D), BF16),
                   jax.ShapeDtypeStruct((T, D), BF16), jax.ShapeDtypeStruct((T, D), BF16),
                   jax.ShapeDtypeStruct((T, D_ATT), BF16), jax.ShapeDtypeStruct((T, D_REC), F32)),
        grid=(T // TM_MIX,),
        in_specs=[tok(D), tok(D_ATT), tok(D_REC), tok(D, 0), tok(D, 1),
                  full((D, D_ATT)), full((D_REC, D)), full((D, D))],
        out_specs=(tok(D), tok(D), tok(D), tok(D), tok(D_ATT), tok(D_REC)),
        compiler_params=_params(dimension_semantics=("parallel",)),
    )(dx1, att, yrec, gg, gg, w_att_o_t, w_rec_o, w_out)


TM_FFN = 256
FF_CHUNK = 1024


def _ffn_loss(x1, target, g2, gf, w_ff1_t, w_ff2):
    n_chunks = D_FF // FF_CHUNK

    def body(x1_ref, tg_ref, g2_ref, gf_ref, w1_hbm, w2_hbm,
             loss_ref, dx1_ref, h2_ref, act_ref, dpre_ref, dx2_ref, dg2_ref, dgf_ref,
             w1, w2, relu_s):
        i = pl.program_id(0)

        @pl.when(i == 0)
        def _():
            pltpu.sync_copy(w1_hbm, w1)
            pltpu.sync_copy(w2_hbm, w2)
            loss_ref[...] = jnp.zeros_like(loss_ref)
            dg2_ref[...] = jnp.zeros_like(dg2_ref)
            dgf_ref[...] = jnp.zeros_like(dgf_ref)

        x1v = x1_ref[...]
        r2 = lax.rsqrt(jnp.mean(x1v * x1v, axis=-1, keepdims=True) + EPS)
        xh2 = x1v * r2
        h2 = (xh2 * g2_ref[...]).astype(BF16)
        h2_ref[...] = h2
        x2 = x1v
        for c in range(n_chunks):
            ff = slice(c * FF_CHUNK, (c + 1) * FF_CHUNK)
            rl = jnp.maximum(_dot_nt(h2, w1[ff, :]), 0.0)
            relu_s[:, ff] = rl
            act = (rl * rl).astype(BF16)
            act_ref[:, ff] = act
            x2 = x2 + _dot(act, w2[ff, :])
        r3 = lax.rsqrt(jnp.mean(x2 * x2, axis=-1, keepdims=True) + EPS)
        xh3 = x2 * r3
        err = xh3 * gf_ref[...] - tg_ref[...]
        loss_ref[...] += 0.5 * jnp.sum(jnp.mean(err * err, axis=-1, keepdims=True))
        dy = err * (1.0 / D)
        dgf_ref[...] += jnp.sum(dy * xh3, axis=0, keepdims=True)
        dx2 = _rms_bwd(dy, xh3, r3, gf_ref[...])
        dx2_16 = dx2.astype(BF16)
        dx2_ref[...] = dx2_16
        dh2 = jnp.zeros((TM_FFN, D), F32)
        for c in range(n_chunks):
            ff = slice(c * FF_CHUNK, (c + 1) * FF_CHUNK)
            dpre = (_dot_nt(dx2_16, w2[ff, :]) * (2.0 * relu_s[:, ff])).astype(BF16)
            dpre_ref[:, ff] = dpre
            dh2 = dh2 + _dot(dpre, w1[ff, :])
        dg2_ref[...] += jnp.sum(dh2 * xh2, axis=0, keepdims=True)
        dx1_ref[...] = dx2 + _rms_bwd(dh2, xh2, r2, g2_ref[...])

    tok = lambda width: pl.BlockSpec((TM_FFN, width), lambda i: (i, 0))
    vec = pl.BlockSpec((1, D), lambda i: (0, 0))
    hbm = pl.BlockSpec(memory_space=pl.ANY)
    return pl.pallas_call(
        body, name="ffn_loss",
        out_shape=(jax.ShapeDtypeStruct((8, 128), F32), jax.ShapeDtypeStruct((T, D), F32),
                   jax.ShapeDtypeStruct((T, D), BF16), jax.ShapeDtypeStruct((T, D_FF), BF16),
                   jax.ShapeDtypeStruct((T, D_FF), BF16), jax.ShapeDtypeStruct((T, D), BF16),
                   jax.ShapeDtypeStruct((1, D), F32), jax.ShapeDtypeStruct((1, D), F32)),
        grid=(T // TM_FFN,),
        in_specs=[tok(D), tok(D), vec, vec, hbm, hbm],
        out_specs=(pl.BlockSpec((8, 128), lambda i: (0, 0)), tok(D), tok(D), tok(D_FF), tok(D_FF), tok(D),
                   vec, vec),
        scratch_shapes=[pltpu.VMEM((D_FF, D), BF16), pltpu.VMEM((D_FF, D), BF16),
                        pltpu.VMEM((TM_FFN, D_FF), F32)],
        compiler_params=_params(dimension_semantics=("arbitrary",)),
    )(x1, target, g2, gf, w_ff1_t, w_ff2)


def _local_step(x, target, p, late_weights, reduce_early):
    bias = _rpb_expand(p["rpb"])
    pairs = lambda w: w.reshape(2, N_CG, CG, REC_BLOCK)
    w_a, w_i = pairs(p["w_rg_a"]), pairs(p["w_rg_i"])
    rec_params = (p["conv_w"], p["conv_b"], w_a, p["b_rg_a"], w_i, p["b_rg_i"], p["lru_lambda"])

    qkv, uy, gg, h = _in_proj(x, p["ln1_g"], p["w_in_t"], p["b_in"])
    att = _att_fwd(qkv, bias)
    hf, hb, yrec = _rec_fwd(uy, *rec_params)
    p = {**p, **late_weights(yrec)}
    x1, mixed = _mix_fwd(x, att, yrec, gg, p["w_att_o_t"], p["w_rec_o"], p["w_out"])
    loss8, dx1, h2, act, dpre, dx2, g_ln2, g_lnf = _ffn_loss(
        x1, target, p["ln2_g"], p["lnf_g"], p["w_ff1_t"], p["w_ff2"])

    dga, dgr, dya, dyr, datt, dyrp = _mix_bwd(dx1, att, yrec, gg, p["w_att_o_t"], p["w_rec_o"], p["w_out"])
    dup, dyb, g_cw, g_cb, g_wa, g_ba, g_wi, g_bi, g_lam = _rec_bwd(uy, hf, hb, dyrp, *rec_params)
    blocks = lambda g: g.reshape(2, N_REC_BLOCKS, REC_BLOCK, REC_BLOCK)
    grads = {
        "w_att_o_t": _matmul(dya, att, "tn", BF16, "g_w_att_o"),
        "conv_w": g_cw, "conv_b": g_cb, "w_rg_a": blocks(g_wa), "b_rg_a": g_ba,
        "w_rg_i": blocks(g_wi), "b_rg_i": g_bi, "lru_lambda": g_lam,
        "w_rec_o": _matmul(yrec, dyr, "tn", BF16, "g_w_rec_o"),
        "w_out": _matmul(mixed, dx1, "tn", BF16, "g_w_out"),
        "ln2_g": g_ln2,
        "w_ff1_t": _matmul(dpre, h2, "tn", BF16, "g_w_ff1"),
        "w_ff2": _matmul(act, dx2, "tn", BF16, "g_w_ff2"),
        "lnf_g": g_lnf,
    }
    after = reduce_early(grads)
    dq, dk, dv, gbias = _att_bwd(qkv, bias, datt, after)
    dz = (dq, dk, dv, dup, dyb, dga, dgr)
    grad_x, g_ln1 = _dh_norm1_bwd(dz, p["w_in_t"], x, p["ln1_g"], dx1)
    g_w_in_t, g_b_in = _grad_w_in(dz, h)
    grads.update(ln1_g=g_ln1, w_in_t=g_w_in_t, b_in=g_b_in, rpb=_rpb_reduce(gbias))
    return loss8[0:1, 0:1], grad_x, grads


MESH_ID = pl.DeviceIdType.MESH
ANY = pl.BlockSpec(memory_space=pl.ANY)

CHAN_BLOCK_ROWS = 32
SECTIONS = (("w_in_t", 704, D), ("w_rec_o", 128, D), ("w_out", 128, D), ("w_ff1_t", 512, D),
            ("w_ff2", 512, D), ("chan", CHAN_BLOCK_ROWS, D), ("w_att_o_t", 128, D_ATT))
N_SEC = len(SECTIONS)
N_CHAN_ROWS = 10
CHAN = (("conv_w", 4), ("b_rg_a", 2), ("b_rg_i", 2), ("lru_lambda", 2))


def _position():
    return lax.axis_index("x"), lax.axis_index("y"), lax.axis_index("c")


def _other_chips(x, y):
    return [(1 - x, y), (x, 1 - y), (1 - x, 1 - y)]


def _block_of(ref, dev, rows):
    return ref.at[pl.ds(pl.multiple_of(dev * rows, 16), rows)]


def _all_gather(shards, name):
    ns = len(shards)

    def body(*refs):
        x_refs, out_refs = refs[:ns], refs[ns:2 * ns]
        send_sems, recv_sems, local_sems = refs[2 * ns:]
        x, y, c = _position()
        me, sibling = (x, y, c), (x, y, 1 - c)
        chips = _other_chips(x, y)

        def rows(s, px, py, pc):
            return _block_of(out_refs[s], 4 * px + 2 * py + pc, shards[s].shape[0])

        def copy(k, s, block, to, from_shard=False):
            return pltpu.make_async_remote_copy(
                src_ref=x_refs[s] if from_shard else rows(s, *block), dst_ref=rows(s, *block),
                send_sem=send_sems.at[k * ns + s], recv_sem=recv_sems.at[k * ns + s],
                device_id=to, device_id_type=MESH_ID)

        sections = range(ns)
        mine = [pltpu.make_async_copy(x_refs[s], rows(s, *me), local_sems.at[s]) for s in sections]
        first = [copy(0, s, me, sibling, True) for s in sections]
        first += [copy(1 + j, s, me, (*chip, c), True) for j, chip in enumerate(chips) for s in sections]
        for cp in mine + first:
            cp.start()
        passed = []
        for j, chip in enumerate(chips):
            for s in sections:
                copy(1 + j, s, (*chip, c), me).wait_recv()
                passed.append(copy(4 + j, s, (*chip, c), sibling))
                passed[-1].start()
        for s in sections:
            copy(0, s, sibling, me).wait_recv()
        for j, chip in enumerate(chips):
            for s in sections:
                copy(4 + j, s, (*chip, 1 - c), me).wait_recv()
        for cp in first + passed:
            cp.wait_send()
        for cp in mine:
            cp.wait()

    return pl.pallas_call(
        body, name=name,
        out_shape=tuple(jax.ShapeDtypeStruct((N_DEV * s.shape[0], s.shape[1]), s.dtype) for s in shards),
        in_specs=[ANY] * ns, out_specs=(ANY,) * ns,
        scratch_shapes=[pltpu.SemaphoreType.DMA((7 * ns,)), pltpu.SemaphoreType.DMA((7 * ns,)),
                        pltpu.SemaphoreType.DMA((ns,))],
    )(*shards)


HBM = pl.BlockSpec(memory_space=pltpu.HBM)
SEM = pl.BlockSpec(memory_space=pltpu.SEMAPHORE)
EFFECT = pltpu.SideEffectType.DATAFLOW_SIDE_EFFECTING


def _in_hbm(a):
    return pltpu.with_memory_space_constraint(a, pltpu.HBM)


def _first_hop_copies(shards, x_refs, zones, send_sems, recv_sems):
    ns = len(shards)
    x, y, c = _position()
    targets = [(x, y, 1 - c)] + [(cx, cy, c) for cx, cy in _other_chips(x, y)]
    return [pltpu.make_async_remote_copy(
        src_ref=x_refs[s], dst_ref=_block_of(zones[s], 4 * x + 2 * y + c, shards[s].shape[0]),
        send_sem=send_sems.at[k * ns + s], recv_sem=recv_sems.at[k * ns + s],
        device_id=to, device_id_type=MESH_ID)
        for k, to in enumerate(targets) for s in range(ns)]


def _gather_start(shards, name):
    ns = len(shards)
    x, y, c = _position()
    me = 4 * x + 2 * y + c
    zones = [lax.dynamic_update_slice(lax.empty((N_DEV * s.shape[0], s.shape[1]), s.dtype), s,
                                      (me * s.shape[0], 0)) for s in shards]

    def body(*refs):
        for cp in _first_hop_copies(shards, refs[:ns], refs[ns:2 * ns], refs[2 * ns], refs[2 * ns + 1]):
            cp.start()
        refs[-1][...] = jnp.zeros_like(refs[-1])

    out = pl.pallas_call(
        body, name=name,
        out_shape=(pltpu.SemaphoreType.DMA((4 * ns,)), pltpu.SemaphoreType.DMA((4 * ns,)),
                   *[pltpu.HBM(a.shape, a.dtype) for a in (*shards, *zones)],
                   jax.ShapeDtypeStruct((8, LANES), F32)),
        in_specs=[HBM] * (2 * ns),
        out_specs=(SEM, SEM, *[HBM] * (2 * ns), pl.BlockSpec(memory_space=pltpu.VMEM)),
        input_output_aliases={i: 2 + i for i in range(2 * ns)},
        compiler_params=pltpu.CompilerParams(has_side_effects=EFFECT),
    )(*[_in_hbm(a) for a in shards], *[_in_hbm(a) for a in zones])
    return out[0], out[1], out[2:2 + ns], out[2 + ns:2 + 2 * ns], out[-1]


def _gather_wait(send_sems, recv_sems, shards, zones, after, name):
    ns = len(shards)

    def body(*refs):
        for cp in _first_hop_copies(shards, refs[:ns], refs[ns:2 * ns], refs[2 * ns], refs[2 * ns + 1]):
            cp.wait_send()
            cp.wait_recv()

    out = pl.pallas_call(
        body, name=name,
        out_shape=tuple(pltpu.HBM(a.shape, a.dtype) for a in (*shards, *zones)),
        in_specs=[HBM] * (2 * ns) + [SEM, SEM, ANY],
        out_specs=(HBM,) * (2 * ns),
        input_output_aliases={i: i for i in range(2 * ns)},
        compiler_params=pltpu.CompilerParams(has_side_effects=EFFECT),
    )(*shards, *zones, send_sems, recv_sems, after)
    return out[ns:]


def _gather_pass_on(rows, zones, name):
    ns = len(zones)

    def body(*refs):
        in_refs, out_refs = refs[:ns], refs[ns:2 * ns]
        send_sems, recv_sems = refs[2 * ns:]
        x, y, c = _position()
        copies = [pltpu.make_async_remote_copy(
            src_ref=_block_of(in_refs[s], 4 * cx + 2 * cy + c, rows[s]),
            dst_ref=_block_of(out_refs[s], 4 * cx + 2 * cy + c, rows[s]),
            send_sem=send_sems.at[j * ns + s], recv_sem=recv_sems.at[j * ns + s],
            device_id=(x, y, 1 - c), device_id_type=MESH_ID)
            for j, (cx, cy) in enumerate(_other_chips(x, y)) for s in range(ns)]
        for cp in copies:
            cp.start()
        for cp in copies:
            cp.wait_recv()
        for cp in copies:
            cp.wait_send()

    return pl.pallas_call(
        body, name=name,
        out_shape=tuple(jax.ShapeDtypeStruct(z.shape, z.dtype) for z in zones),
        in_specs=[ANY] * ns, out_specs=(ANY,) * ns,
        input_output_aliases={i: i for i in range(ns)},
        scratch_shapes=[pltpu.SemaphoreType.DMA((3 * ns,)), pltpu.SemaphoreType.DMA((3 * ns,))],
    )(*zones)


def _pair_exchange(sections, grads, name):
    ns = len(sections)

    def body(*refs):
        g_refs, land = refs[:ns], refs[ns:2 * ns]
        send_sems, recv_sems = refs[2 * ns:]
        x, y, c = _position()
        copies = [pltpu.make_async_remote_copy(
            src_ref=_block_of(g_refs[s], 2 * k + 1 - c, rows), dst_ref=land[s].at[k],
            send_sem=send_sems.at[k * ns + s], recv_sem=recv_sems.at[k * ns + s],
            device_id=(x, y, 1 - c), device_id_type=MESH_ID)
            for k in range(N_CHIPS) for s, (_, rows, _) in enumerate(sections)]
        for cp in copies:
            cp.start()
        for cp in copies:
            cp.wait_recv()
        for cp in copies:
            cp.wait_send()

    n = N_CHIPS * ns
    return pl.pallas_call(
        body, name=name,
        out_shape=tuple(jax.ShapeDtypeStruct((N_CHIPS, rows, cols), BF16) for _, rows, cols in sections),
        in_specs=[ANY] * ns, out_specs=(ANY,) * ns,
        scratch_shapes=[pltpu.SemaphoreType.DMA((n,)), pltpu.SemaphoreType.DMA((n,))],
    )(*grads)


def _pair_add(sections, grads, got, core, name):
    ns = len(sections)

    def body(core_ref, *refs):
        g_refs, got_refs, p_refs = refs[:ns], refs[ns:2 * ns], refs[2 * ns:]
        for s in range(ns):
            p_refs[s][0] = (g_refs[s][...].astype(F32) + got_refs[s][0].astype(F32)).astype(BF16)

    slot = [pl.BlockSpec((1, rows, cols), lambda k, c: (k, 0, 0)) for _, rows, cols in sections]
    return pl.pallas_call(
        body, name=name,
        out_shape=tuple(jax.ShapeDtypeStruct((N_CHIPS, rows, cols), BF16) for _, rows, cols in sections),
        grid_spec=pltpu.PrefetchScalarGridSpec(
            num_scalar_prefetch=1, grid=(N_CHIPS,),
            in_specs=[pl.BlockSpec((rows, cols), lambda k, c: (2 * k + c[0], 0)) for _, rows, cols in sections]
            + slot,
            out_specs=tuple(slot)),
        compiler_params=_params(dimension_semantics=("parallel",)),
    )(core, *grads, *got)


def _chip_copies(sections, p_refs, land, send_sems, recv_sems):
    ns = len(sections)
    x, y, c = _position()
    return [pltpu.make_async_remote_copy(
        src_ref=p_refs[s].at[2 * cx + cy], dst_ref=land[s].at[j],
        send_sem=send_sems.at[j * ns + s], recv_sem=recv_sems.at[j * ns + s],
        device_id=(cx, cy, c), device_id_type=MESH_ID)
        for j, (cx, cy) in enumerate(_other_chips(x, y)) for s in range(ns)]


def _chip_exchange(sections, parts, name):
    ns = len(sections)

    def body(*refs):
        copies = _chip_copies(sections, refs[:ns], refs[ns:2 * ns], *refs[2 * ns:])
        for cp in copies:
            cp.start()
        for cp in copies:
            cp.wait_recv()
        for cp in copies:
            cp.wait_send()

    n = 3 * ns
    return pl.pallas_call(
        body, name=name,
        out_shape=tuple(jax.ShapeDtypeStruct((3, rows, cols), BF16) for _, rows, cols in sections),
        in_specs=[ANY] * ns, out_specs=(ANY,) * ns,
        scratch_shapes=[pltpu.SemaphoreType.DMA((n,)), pltpu.SemaphoreType.DMA((n,))],
    )(*parts)


def _chip_exchange_start(sections, parts, name):
    ns = len(sections)

    def body(*refs):
        p_refs, land = refs[:ns], refs[ns:2 * ns]
        send_sems, recv_sems = refs[2 * ns], refs[2 * ns + 1]
        token = refs[-1]
        for cp in _chip_copies(sections, p_refs, land, send_sems, recv_sems):
            cp.start()
        token[...] = jnp.zeros_like(token)

    zones = [lax.empty((3, rows, cols), BF16) for _, rows, cols in sections]
    out = pl.pallas_call(
        body, name=name,
        out_shape=(pltpu.SemaphoreType.DMA((3 * ns,)), pltpu.SemaphoreType.DMA((3 * ns,)),
                   *[pltpu.HBM(a.shape, a.dtype) for a in parts], *[pltpu.HBM(a.shape, a.dtype) for a in zones],
                   jax.ShapeDtypeStruct((8, LANES), F32)),
        in_specs=[HBM] * (2 * ns),
        out_specs=(SEM, SEM, *[HBM] * (2 * ns), pl.BlockSpec(memory_space=pltpu.VMEM)),
        input_output_aliases={i: 2 + i for i in range(2 * ns)},
        compiler_params=pltpu.CompilerParams(has_side_effects=EFFECT),
    )(*[_in_hbm(a) for a in parts], *[_in_hbm(a) for a in zones])
    return out[0], out[1], out[2:2 + ns], out[2 + ns:2 + 2 * ns], out[-1]


def _chip_exchange_wait(sections, send_sems, recv_sems, parts, zones, after, name):
    ns = len(sections)

    def body(*refs):
        p_refs, land = refs[:ns], refs[ns:2 * ns]
        for cp in _chip_copies(sections, p_refs, land, refs[2 * ns], refs[2 * ns + 1]):
            cp.wait_send()
            cp.wait_recv()

    out = pl.pallas_call(
        body, name=name,
        out_shape=tuple(pltpu.HBM(a.shape, a.dtype) for a in (*parts, *zones)),
        in_specs=[HBM] * (2 * ns) + [SEM, SEM, ANY],
        out_specs=(HBM,) * (2 * ns),
        input_output_aliases={i: i for i in range(2 * ns)},
        compiler_params=pltpu.CompilerParams(has_side_effects=EFFECT),
    )(*parts, *zones, send_sems, recv_sems, after)
    return out[:ns], out[ns:]


def _grad_finish(parts, far, chip):
    def body(chip_ref, *refs):
        p_refs, b_refs, g_refs = refs[:N_SEC], refs[N_SEC:2 * N_SEC], refs[2 * N_SEC:]
        for s in range(N_SEC):
            g = p_refs[s][0].astype(F32)
            for j in range(3):
                g = g + b_refs[s][j].astype(F32)
            g_refs[s][...] = g

    half = [(rows // 2, cols) for _, rows, cols in SECTIONS]
    return pl.pallas_call(
        body, name="grad_finish",
        out_shape=tuple(jax.ShapeDtypeStruct((rows, cols), F32) for _, rows, cols in SECTIONS),
        grid_spec=pltpu.PrefetchScalarGridSpec(
            num_scalar_prefetch=1, grid=(2,),
            in_specs=[pl.BlockSpec((1, r, c), lambda i, chip: (chip[0], i, 0)) for r, c in half]
            + [pl.BlockSpec((3, r, c), lambda i, chip: (0, i, 0)) for r, c in half],
            out_specs=tuple(pl.BlockSpec((r, c), lambda i, chip: (i, 0)) for r, c in half)),
        compiler_params=_params(dimension_semantics=("parallel",)),
    )(chip, *parts, *far)


def _sum_devices(parts, rows):
    tr = rows // 2

    def body(*refs):
        s = refs[0][...]
        for d in range(1, N_DEV):
            s = s + refs[d][...]
        refs[N_DEV][...] = s

    return pl.pallas_call(
        body, name="small_grad_sum",
        out_shape=jax.ShapeDtypeStruct((rows, LANES), F32),
        grid=(2,),
        in_specs=[pl.BlockSpec((tr, LANES), lambda i, d=d: (2 * d + i, 0)) for d in range(N_DEV)],
        out_specs=pl.BlockSpec((tr, LANES), lambda i: (i, 0)),
        compiler_params=_params(dimension_semantics=("parallel",)),
    )(*([parts] * N_DEV))


def _adamw(w, g, m, v, name):
    rows, cols = w.shape
    tr = rows
    while tr * cols * 4 > (1 << 20) and tr % 16 == 0:
        tr //= 2
    c1 = 1.0 / (1.0 - ADAM_B1 ** ADAM_STEP)
    c2 = 1.0 / (1.0 - ADAM_B2 ** ADAM_STEP)

    def body(w_ref, g_ref, m_ref, v_ref, d_ref, nm_ref, nv_ref):
        gv = g_ref[...]
        nm = ADAM_B1 * m_ref[...] + (1.0 - ADAM_B1) * gv
        nv = ADAM_B2 * v_ref[...] + (1.0 - ADAM_B2) * (gv * gv)
        nm_ref[...] = nm
        nv_ref[...] = nv
        d_ref[...] = (-ADAM_LR) * ((nm * c1) / (jnp.sqrt(nv * c2) + ADAM_EPS) + ADAM_WD * w_ref[...])

    spec = pl.BlockSpec((tr, cols), lambda i: (i, 0))
    shape = jax.ShapeDtypeStruct((rows, cols), F32)
    return pl.pallas_call(
        body, name=name,
        out_shape=(shape, shape, shape),
        grid=(rows // tr,),
        in_specs=[spec] * 4, out_specs=(spec,) * 3,
        compiler_params=_params(dimension_semantics=("parallel",)),
    )(w, g, m, v)


NAMES = ("ln1_g", "w_in", "b_in", "rpb", "w_att_o", "conv_w", "conv_b", "w_rg_a", "b_rg_a", "w_rg_i",
         "b_rg_i", "lru_lambda", "w_rec_o", "w_out", "ln2_g", "w_ff1", "w_ff2", "lnf_g")
TRANSPOSED = {"w_in": "w_in_t", "w_att_o": "w_att_o_t", "w_ff1": "w_ff1_t"}
ROW_SHARDED = ("w_rec_o", "w_out", "w_ff2")
REPLICATED = (("ln1_g", (1, D)), ("b_in", (1, D_IN)), ("rpb", (N_HEADS * N_RPB_R, N_RPB_C)),
              ("conv_b", (1, D_REC)), ("w_rg_a", (2 * N_REC_BLOCKS * REC_BLOCK, REC_BLOCK)),
              ("w_rg_i", (2 * N_REC_BLOCKS * REC_BLOCK, REC_BLOCK)), ("ln2_g", (1, D)), ("lnf_g", (1, D)))
SMALL_ROWS = 2160


def _chan_bits(vectors):
    chan = jnp.concatenate(vectors, axis=0)
    bits = lax.bitcast_convert_type(chan, BF16).reshape(-1)
    return jnp.pad(bits, (0, CHAN_BLOCK_ROWS * D - bits.shape[0])).reshape(CHAN_BLOCK_ROWS, D)


def _chan_from_bits(gathered):
    bits = gathered.reshape(N_DEV, CHAN_BLOCK_ROWS * D)[:, :2 * N_CHAN_ROWS * LANES]
    chan = lax.bitcast_convert_type(bits.reshape(N_DEV, N_CHAN_ROWS, LANES, 2), F32)
    return chan.transpose(1, 0, 2).reshape(N_CHAN_ROWS, D)


def kernel(x, ln1_g, w_in, b_in, rpb, w_att_o, conv_w, conv_b, w_rg_a, b_rg_a, w_rg_i, b_rg_i, lru_lambda, w_rec_o, w_out, ln2_g, w_ff1, w_ff2, lnf_g, loss_target, m_ln1_g, m_w_in, m_b_in, m_rpb, m_w_att_o, m_conv_w, m_conv_b, m_w_rg_a, m_b_rg_a, m_w_rg_i, m_b_rg_i, m_lru_lambda, m_w_rec_o, m_w_out, m_ln2_g, m_w_ff1, m_w_ff2, m_lnf_g, v_ln1_g, v_w_in, v_b_in, v_rpb, v_w_att_o, v_conv_w, v_conv_b, v_w_rg_a, v_b_rg_a, v_w_rg_i, v_b_rg_i, v_lru_lambda, v_w_rec_o, v_w_out, v_ln2_g, v_w_ff1, v_w_ff2, v_lnf_g):
    w = dict(zip(NAMES, (ln1_g, w_in, b_in, rpb, w_att_o, conv_w, conv_b, w_rg_a, b_rg_a, w_rg_i,
                         b_rg_i, lru_lambda, w_rec_o, w_out, ln2_g, w_ff1, w_ff2, lnf_g)))
    m = dict(zip(NAMES, (m_ln1_g, m_w_in, m_b_in, m_rpb, m_w_att_o, m_conv_w, m_conv_b, m_w_rg_a,
                         m_b_rg_a, m_w_rg_i, m_b_rg_i, m_lru_lambda, m_w_rec_o, m_w_out, m_ln2_g,
                         m_w_ff1, m_w_ff2, m_lnf_g)))
    v = dict(zip(NAMES, (v_ln1_g, v_w_in, v_b_in, v_rpb, v_w_att_o, v_conv_w, v_conv_b, v_w_rg_a,
                         v_b_rg_a, v_w_rg_i, v_b_rg_i, v_lru_lambda, v_w_rec_o, v_w_out, v_ln2_g,
                         v_w_ff1, v_w_ff2, v_lnf_g)))
    xi, yi, ci = _position()

    shard = {t: w[n][0].T.astype(BF16) for n, t in TRANSPOSED.items()}
    shard.update({n: w[n][0].astype(BF16) for n in ROW_SHARDED})
    shard["chan"] = _chan_bits([w[n][0] for n, _ in CHAN])
    first, later = ("w_in_t", "chan"), ("w_rec_o", "w_out", "w_ff1_t", "w_ff2", "w_att_o_t")
    p = dict(zip(first, _all_gather([shard[n] for n in first], "weight_all_gather")))
    send_sems, recv_sems, sent, zones, token = _gather_start([shard[n] for n in later], "weight_gather_start")

    def late_weights(after):
        landed = _gather_wait(send_sems, recv_sems, sent, zones, after, "weight_gather_wait")
        return dict(zip(later, _gather_pass_on([shard[n].shape[0] for n in later], landed,
                                               "weight_gather_pass_on")))

    chan = _chan_from_bits(p.pop("chan"))
    r0 = 0
    for n, rows in CHAN:
        p[n] = chan[r0:r0 + rows]
        r0 += rows
    p.update(ln1_g=w["ln1_g"], b_in=w["b_in"] + token[0, 0], rpb=w["rpb"][0], conv_b=w["conv_b"],
             w_rg_a=w["w_rg_a"][0], w_rg_i=w["w_rg_i"][0], ln2_g=w["ln2_g"],
             lnf_g=w["lnf_g"].reshape(1, D))

    core = jnp.reshape(ci, (1,)).astype(jnp.int32)
    chip = jnp.reshape(2 * xi + yi, (1,)).astype(jnp.int32)
    early_sections, late_sections = SECTIONS[1:], SECTIONS[:1]
    in_flight = {}

    def reduce_early(grads):
        chan_g = jnp.concatenate([grads[n] for n, _ in CHAN], axis=0)
        chan_g = chan_g.reshape(N_CHAN_ROWS, N_DEV, LANES).transpose(1, 0, 2).astype(BF16)
        chan_g = jnp.pad(chan_g.reshape(N_DEV, -1), ((0, 0), (0, CHAN_BLOCK_ROWS * D - N_CHAN_ROWS * LANES)))
        grads["chan"] = chan_g.reshape(N_DEV * CHAN_BLOCK_ROWS, D)
        sect = [grads[n] for n, _, _ in early_sections]
        got = _pair_exchange(early_sections, sect, "grad_pair_exchange_early")
        parts = _pair_add(early_sections, sect, got, core, "grad_pair_add_early")
        in_flight["early"] = _chip_exchange_start(early_sections, parts, "grad_chip_exchange_start")
        return in_flight["early"][-1][0, 0]

    loss_part, grad_x, grads = _local_step(x[0], loss_target[0], p, late_weights, reduce_early)
    sect = [grads[n] for n, _, _ in late_sections]
    got = _pair_exchange(late_sections, sect, "grad_pair_exchange_late")
    late_parts = _pair_add(late_sections, sect, got, core, "grad_pair_add_late")
    late_far = _chip_exchange(late_sections, late_parts, "grad_chip_exchange_late")
    send_sems, recv_sems, early_parts, zones, _ = in_flight["early"]
    early_parts, early_far = _chip_exchange_wait(early_sections, send_sems, recv_sems, early_parts, zones,
                                                 late_far[0], "grad_chip_exchange_wait")
    summed = dict(zip((n for n, _, _ in SECTIONS),
                      _grad_finish([*late_parts, *early_parts], [*late_far, *early_far], chip)))


    flat = jnp.concatenate([grads[n].reshape(-1) for n, _ in REPLICATED] + [loss_part.reshape(-1)])
    n_small = flat.shape[0]
    flat = jnp.pad(flat, (0, SMALL_ROWS * LANES - n_small)).reshape(SMALL_ROWS, LANES)
    (small_parts,) = _all_gather([flat], "small_grad_all_gather")
    small = _sum_devices(small_parts, SMALL_ROWS).reshape(-1)
    loss = small[n_small - 1]

    g, delta, new_m, new_v = {}, {}, {}, {}

    def update(n, g2, shape2):
        d2, m2, v2 = _adamw(w[n].reshape(shape2), g2, m[n].reshape(shape2), v[n].reshape(shape2),
                            "adamw_" + n)
        g[n], delta[n], new_m[n], new_v[n] = (a.reshape(w[n].shape) for a in (g2, d2, m2, v2))

    for n in ROW_SHARDED:
        update(n, summed[n], summed[n].shape)
    for n, t in TRANSPOSED.items():
        update(n, summed[t].T, summed[t].shape[::-1])
    chan_back = summed["chan"].reshape(-1)[:N_CHAN_ROWS * LANES].reshape(N_CHAN_ROWS, LANES)
    r0 = 0
    for n, rows in CHAN:
        update(n, chan_back[r0:r0 + rows], (rows, LANES))
        r0 += rows
    o = 0
    for n, shape2 in REPLICATED:
        size = shape2[0] * shape2[1]
        update(n, small[o:o + size].reshape(shape2), shape2)
        o += size

    return (loss, grad_x[None], *[g[n] for n in NAMES], *[delta[n] for n in NAMES],
            *[new_m[n] for n in NAMES], *[new_v[n] for n in NAMES])
```

```python
import math

import numpy as np
import jax
import jax.numpy as jnp
from jax import lax
from jax.experimental import pallas as pl
from jax.experimental.pallas import tpu as pltpu

F32 = jnp.float32
BF16 = jnp.bfloat16

T = 2048
D = 1024
D_ATT = 512
D_REC = 1024
D_FF = 4096
D_IN = 5632
N_HEADS = 8
DH = 64
GRID_W = 64
ROWS = T // GRID_W
WIN_H = 8
WIN_W = 16
KWIN = WIN_H * GRID_W
N_RPB_R = 2 * WIN_H - 1
N_RPB_C = 2 * WIN_W - 1
N_REC_BLOCKS = 16
REC_BLOCK = 64
CG = 128
N_CG = D_REC // CG
LRU_C = 8.0
EPS = 1e-6
N_DEV = 8
N_CHIPS = 4
LANES = 128

ADAM_LR = 0.001
ADAM_B1 = 0.9
ADAM_B2 = 0.999
ADAM_EPS = 1e-08
ADAM_WD = 0.01
ADAM_STEP = 10

MESH_AXES = ("x", "y", "c")
VMEM_LIMIT = 56 * 1024 * 1024

TILE = 512
DZ_ARRAYS = ((0, 3, 1), (3, 4, 2), (7, 4, 2))
N_DZ_TILES = D_IN // TILE


def _params(**kw):
    return pltpu.CompilerParams(vmem_limit_bytes=VMEM_LIMIT, **kw)


def _att_tables():
    rq = np.arange(2 * GRID_W) % GRID_W
    kc = np.arange(KWIN) % GRID_W
    win_start = np.clip(rq - WIN_W // 2, 0, GRID_W - WIN_W)
    valid = (kc[None, :] >= win_start[:, None]) & (kc[None, :] < win_start[:, None] + WIN_W)
    return valid.astype(np.float32), _pair_mask()


def _pair_mask():
    half = np.arange(2 * DH) // DH
    return (half[:, None] == half[None, :]).astype(np.float32)


def _dup_table():
    return np.concatenate([np.eye(REC_BLOCK, dtype=np.float32)] * 2, axis=1)


def _toeplitz_table():
    q = np.arange(GRID_W)[:, None]
    kc = np.arange(GRID_W)[None, :]
    dc = (kc - q + WIN_W - 1).reshape(-1)
    e = np.zeros((128, GRID_W * GRID_W), np.float32)
    ok = (dc >= 0) & (dc < N_RPB_C)
    e[dc[ok], np.arange(GRID_W * GRID_W)[ok]] = 1.0
    return e


def _row_shift_table():
    s = np.zeros((16, WIN_H * WIN_H), np.float32)
    for oi in range(WIN_H):
        for i in range(WIN_H):
            s[i - oi + WIN_H - 1, oi * WIN_H + i] = 1.0
    return s


def _sigmoid(x):
    return 0.5 * jnp.tanh(0.5 * x) + 0.5


def _softplus(x):
    return jnp.maximum(x, 0.0) + jnp.log(1.0 + jnp.exp(-jnp.abs(x)))


def _one_minus_square(log_a, a):
    x = 2.0 * log_a
    series = -x * (1.0 + x * (0.5 + x * (1.0 / 6.0)))
    return jnp.where(x > -0.02, series, 1.0 - a * a)


_GELU_C = math.sqrt(2.0 / math.pi)


def _gelu_and_grad(x):
    x2 = x * x
    inner = _GELU_C * (x + 0.044715 * x * x2)
    t = jnp.tanh(inner)
    g = 0.5 * x * (1.0 + t)
    dg = 0.5 * (1.0 + t) + 0.5 * x * (1.0 - t * t) * _GELU_C * (1.0 + 3.0 * 0.044715 * x2)
    return g, dg


def _dot(a, b):
    return jnp.dot(a, b, preferred_element_type=F32)


def _dot_nt(a, b):
    return lax.dot_general(a, b, (((1,), (1,)), ((), ())), preferred_element_type=F32)


def _dot_tn(a, b):
    return lax.dot_general(a, b, (((0,), (0,)), ((), ())), preferred_element_type=F32)


def _dot_exact(a, b):
    return jnp.dot(a, b, precision=lax.Precision.HIGHEST, preferred_element_type=F32)


def _shift_rows(x, s):
    n = x.shape[0]
    rows = lax.broadcasted_iota(jnp.int32, x.shape, 0)
    y = pltpu.roll(x, s % n, 0)
    if s > 0:
        return jnp.where(rows >= s, y, 0.0)
    return jnp.where(rows < n + s, y, 0.0)


def _rms_bwd(dh, xh, r, g):
    dxh = dh * g
    return r * (dxh - xh * jnp.mean(dxh * xh, axis=-1, keepdims=True))


def _matmul(a, b, mode, out_dtype, name, tm=512, tn=1024, tk=2048):
    if mode == "nn":
        (m, k), (k2, n) = a.shape, b.shape
    elif mode == "nt":
        (m, k), (n, k2) = a.shape, b.shape
    else:
        (k, m), (k2, n) = a.shape, b.shape
    assert k == k2
    tm, tn, tk = min(tm, m), min(tn, n), min(tk, k)
    assert m % tm == 0 and n % tn == 0 and k % tk == 0
    nk = k // tk
    dot = {"nn": _dot, "nt": _dot_nt, "tn": _dot_tn}[mode]

    def body(a_ref, b_ref, o_ref, acc):
        kk = pl.program_id(2)
        part = dot(a_ref[...].astype(BF16), b_ref[...].astype(BF16))
        if nk == 1:
            o_ref[...] = part.astype(out_dtype)
            return

        @pl.when(kk == 0)
        def _():
            acc[...] = part

        @pl.when(kk > 0)
        def _():
            acc[...] += part

        @pl.when(kk == nk - 1)
        def _():
            o_ref[...] = acc[...].astype(out_dtype)

    if mode == "tn":
        a_spec = pl.BlockSpec((tk, tm), lambda i, j, kk: (kk, i))
    else:
        a_spec = pl.BlockSpec((tm, tk), lambda i, j, kk: (i, kk))
    if mode == "nt":
        b_spec = pl.BlockSpec((tn, tk), lambda i, j, kk: (j, kk))
    else:
        b_spec = pl.BlockSpec((tk, tn), lambda i, j, kk: (kk, j))
    return pl.pallas_call(
        body, name=name,
        out_shape=jax.ShapeDtypeStruct((m, n), out_dtype),
        grid=(m // tm, n // tn, nk),
        in_specs=[a_spec, b_spec],
        out_specs=pl.BlockSpec((tm, tn), lambda i, j, kk: (i, j)),
        scratch_shapes=[pltpu.VMEM((tm, tn) if nk > 1 else (8, LANES), F32)],
        compiler_params=_params(dimension_semantics=("parallel", "parallel", "arbitrary")),
    )(a, b)


def _in_proj(x, g1, w_in_t, b_in):
    tm = 1024

    def body(x_ref, g_ref, w_ref, b_ref, qkv_ref, uy_ref, gg_ref, h_ref, h_scr):
        j = pl.program_id(1)

        @pl.when(j == 0)
        def _():
            xv = x_ref[...]
            r = lax.rsqrt(jnp.mean(xv * xv, axis=-1, keepdims=True) + EPS)
            h = ((xv * r) * g_ref[...]).astype(BF16)
            h_scr[...] = h
            h_ref[...] = h

        z = _dot_nt(h_scr[...], w_ref[...]) + b_ref[...]

        @pl.when(j < 3)
        def _():
            qkv_ref[...] = z.astype(BF16)

        @pl.when((j >= 3) & (j < 7))
        def _():
            uy_ref[...] = z

        @pl.when(j >= 7)
        def _():
            gg_ref[...] = z

    return pl.pallas_call(
        body, name="in_proj",
        out_shape=(jax.ShapeDtypeStruct((T, 3 * D_ATT), BF16),
                   jax.ShapeDtypeStruct((T, 2 * D_REC), F32),
                   jax.ShapeDtypeStruct((T, 2 * D), F32),
                   jax.ShapeDtypeStruct((T, D), BF16)),
        grid=(T // tm, N_DZ_TILES),
        in_specs=[pl.BlockSpec((tm, D), lambda i, j: (i, 0)),
                  pl.BlockSpec((1, D), lambda i, j: (0, 0)),
                  pl.BlockSpec((TILE, D), lambda i, j: (j, 0)),
                  pl.BlockSpec((1, TILE), lambda i, j: (0, j))],
        out_specs=(pl.BlockSpec((tm, TILE), lambda i, j: (i, jnp.minimum(j, 2))),
                   pl.BlockSpec((tm, TILE), lambda i, j: (i, jnp.clip(j - 3, 0, 3))),
                   pl.BlockSpec((tm, TILE), lambda i, j: (i, jnp.clip(j - 7, 0, 3))),
                   pl.BlockSpec((tm, D), lambda i, j: (i, 0))),
        scratch_shapes=[pltpu.VMEM((tm, D), BF16)],
        compiler_params=_params(dimension_semantics=("parallel", "arbitrary")),
    )(x, g1, w_in_t, b_in)


def _dz_specs(rows, tile_of, row_of):
    def spec(off, n, per_plane):
        def index(*ids):
            t = jnp.clip(tile_of(*ids) - off, 0, n - 1)
            return (t // per_plane, row_of(*ids), t % per_plane)
        return pl.BlockSpec((1, rows, TILE), index)
    return [spec(off, n, per) for off, n, per in DZ_ARRAYS]


def _dh_norm1_bwd(dz, w_in_t, x, g1, dx1):
    tm = 1024

    def body(*refs):
        seg_refs = refs[:3]
        w_ref, x_ref, g_ref, dx1_ref, gx_ref, dg_ref, acc = refs[3:]
        i, kk = pl.program_id(0), pl.program_id(1)

        @pl.when(kk == 0)
        def _():
            acc[...] = jnp.zeros_like(acc)

        for s, (off, n, _) in enumerate(DZ_ARRAYS):
            @pl.when((kk >= off) & (kk < off + n))
            def _(s=s):
                acc[...] += _dot(seg_refs[s][0], w_ref[...])

        @pl.when((i == 0) & (kk == 0))
        def _():
            dg_ref[...] = jnp.zeros_like(dg_ref)

        @pl.when(kk == N_DZ_TILES - 1)
        def _():
            xv = x_ref[...]
            r = lax.rsqrt(jnp.mean(xv * xv, axis=-1, keepdims=True) + EPS)
            xh = xv * r
            dh = acc[...]
            dg_ref[...] += jnp.sum(dh * xh, axis=0, keepdims=True)
            gx_ref[...] = dx1_ref[...] + _rms_bwd(dh, xh, r, g_ref[...])

    tok = pl.BlockSpec((tm, D), lambda i, j: (i, 0))
    vec = pl.BlockSpec((1, D), lambda i, j: (0, 0))
    return pl.pallas_call(
        body, name="dh_norm1_bwd",
        out_shape=(jax.ShapeDtypeStruct((T, D), F32), jax.ShapeDtypeStruct((1, D), F32)),
        grid=(T // tm, N_DZ_TILES),
        in_specs=_dz_specs(tm, lambda i, j: j, lambda i, j: i)
        + [pl.BlockSpec((TILE, D), lambda i, j: (j, 0)), tok, vec, tok],
        out_specs=(tok, vec),
        scratch_shapes=[pltpu.VMEM((tm, D), F32)],
        compiler_params=_params(dimension_semantics=("arbitrary", "arbitrary")),
    )(*dz, w_in_t, x, g1, dx1)


def _grad_w_in(dz, h):
    def body(*refs):
        seg_refs = refs[:3]
        h_ref, gw_ref, gb_ref = refs[3:]
        j = pl.program_id(0)

        for s, (off, n, _) in enumerate(DZ_ARRAYS):
            @pl.when((j >= off) & (j < off + n))
            def _(s=s):
                a = seg_refs[s][0]
                gw_ref[...] = _dot_tn(a, h_ref[...]).astype(BF16)
                gb_ref[...] = jnp.sum(a.astype(F32), axis=0, keepdims=True)

    return pl.pallas_call(
        body, name="grad_w_in",
        out_shape=(jax.ShapeDtypeStruct((D_IN, D), BF16), jax.ShapeDtypeStruct((1, D_IN), F32)),
        grid=(N_DZ_TILES,),
        in_specs=_dz_specs(T, lambda j: j, lambda j: 0) + [pl.BlockSpec((T, D), lambda j: (0, 0))],
        out_specs=(pl.BlockSpec((TILE, D), lambda j: (j, 0)), pl.BlockSpec((1, TILE), lambda j: (0, j))),
        compiler_params=_params(dimension_semantics=("parallel",)),
    )(*dz, h)


def _rpb_expand(rpb):
    rpb2 = jnp.pad(rpb.reshape(N_HEADS * N_RPB_R, N_RPB_C), ((0, 0), (0, 128 - N_RPB_C)))
    table = jnp.asarray(_toeplitz_table())

    def body(r_ref, e_ref, o_ref):
        o_ref[...] = _dot_exact(r_ref[...], e_ref[...])

    tb = pl.pallas_call(
        body, name="rpb_expand",
        out_shape=jax.ShapeDtypeStruct((N_HEADS * N_RPB_R, GRID_W * GRID_W), F32),
    )(rpb2, table)
    tb = tb.reshape(N_HEADS, N_RPB_R, GRID_W, GRID_W)
    variants = []
    for oi in range(WIN_H):
        sl = tb[:, WIN_H - 1 - oi: 2 * WIN_H - 1 - oi]
        sl = sl.transpose(0, 2, 1, 3).reshape(N_HEADS // 2, 2 * GRID_W, KWIN)
        variants.append(sl)
    return jnp.stack(variants, axis=0)


def _rpb_reduce(gbias):
    g = gbias.reshape(N_HEADS // 2, WIN_H, 2, GRID_W, WIN_H, GRID_W)
    g = g.transpose(0, 2, 1, 4, 3, 5).reshape(N_HEADS, WIN_H * WIN_H, GRID_W * GRID_W)
    table = jnp.asarray(_toeplitz_table().T.copy())
    shift = jnp.asarray(_row_shift_table())

    def body(g_ref, e_ref, s_ref, o_ref):
        r = _dot_exact(g_ref[0], e_ref[...])
        o_ref[0] = _dot_exact(s_ref[...], r)

    out = pl.pallas_call(
        body, name="rpb_reduce",
        out_shape=jax.ShapeDtypeStruct((N_HEADS, 16, 128), F32),
        grid=(N_HEADS,),
        in_specs=[pl.BlockSpec((1, WIN_H * WIN_H, GRID_W * GRID_W), lambda h: (h, 0, 0)),
                  pl.BlockSpec((GRID_W * GRID_W, 128), lambda h: (0, 0)),
                  pl.BlockSpec((16, WIN_H * WIN_H), lambda h: (0, 0))],
        out_specs=pl.BlockSpec((1, 16, 128), lambda h: (h, 0, 0)),
        compiler_params=_params(dimension_semantics=("arbitrary",)),
    )(g, table, shift)
    return out[:, :N_RPB_R, :N_RPB_C]


def _att_scores(q_ref, k_ref, bias_ref, valid, hmask, r):
    rs = jnp.clip(r - WIN_H // 2, 0, ROWS - WIN_H)
    oi = r - rs
    q0 = pl.multiple_of(r * GRID_W, GRID_W)
    k0 = pl.multiple_of(rs * GRID_W, GRID_W)
    q_r = q_ref[pl.ds(q0, GRID_W), :]
    q2 = jnp.where(hmask, jnp.concatenate([q_r, q_r], axis=0), jnp.zeros((), BF16))
    kw = k_ref[pl.ds(k0, KWIN), :]
    s = _dot_nt(q2, kw) * (DH ** -0.5) + bias_ref[oi, 0]
    s = jnp.where(valid, s, -1e30)
    m = jnp.max(s, axis=-1, keepdims=True)
    p = jnp.exp(s - m)
    p = p / jnp.sum(p, axis=-1, keepdims=True)
    return p, q2, kw, q0, k0, oi


def _att_fwd(qkv, bias):
    valid_np, hmask_np = _att_tables()

    def body(q_ref, k_ref, v_ref, bias_ref, valid_ref, hmask_ref, o_ref):
        valid = valid_ref[...] > 0.5
        hmask = hmask_ref[...] > 0.5
        first_head = lax.broadcasted_iota(jnp.int32, (GRID_W, 2 * DH), 1) < DH

        def row(r, carry):
            p, _, _, q0, k0, _ = _att_scores(q_ref, k_ref, bias_ref, valid, hmask, r)
            o2 = _dot(p.astype(BF16), v_ref[pl.ds(k0, KWIN), :])
            o_ref[pl.ds(q0, GRID_W), :] = jnp.where(first_head, o2[:GRID_W], o2[GRID_W:]).astype(BF16)
            return carry

        lax.fori_loop(0, ROWS, row, 0, unroll=4)

    col = lambda off: pl.BlockSpec((T, 2 * DH), lambda hp: (0, hp + off))
    return pl.pallas_call(
        body, name="att_fwd",
        out_shape=jax.ShapeDtypeStruct((T, D_ATT), BF16),
        grid=(N_HEADS // 2,),
        in_specs=[col(0), col(4), col(8),
                  pl.BlockSpec((WIN_H, 1, 2 * GRID_W, KWIN), lambda hp: (0, hp, 0, 0)),
                  pl.BlockSpec((2 * GRID_W, KWIN), lambda hp: (0, 0)),
                  pl.BlockSpec((2 * DH, 2 * DH), lambda hp: (0, 0))],
        out_specs=pl.BlockSpec((T, 2 * DH), lambda hp: (0, hp)),
        compiler_params=_params(dimension_semantics=("parallel",)),
    )(qkv, qkv, qkv, bias, jnp.asarray(valid_np), jnp.asarray(hmask_np))


def _att_bwd(qkv, bias, datt, after):
    valid_np, hmask_np = _att_tables()

    def body(q_ref, k_ref, v_ref, do_ref, bias_ref, valid_ref, hmask_ref,
             dqkv_ref, gb_ref, dk_acc, dv_acc):
        valid = valid_ref[...] > 0.5
        hmask = hmask_ref[...] > 0.5
        first_head = lax.broadcasted_iota(jnp.int32, (GRID_W, 2 * DH), 1) < DH
        dk_acc[...] = jnp.zeros_like(dk_acc)
        dv_acc[...] = jnp.zeros_like(dv_acc)
        gb_ref[...] = jnp.zeros_like(gb_ref)

        def row(r, carry):
            p, q2, kw, q0, k0, oi = _att_scores(q_ref, k_ref, bias_ref, valid, hmask, r)
            do_r = do_ref[pl.ds(q0, GRID_W), :]
            do2 = jnp.where(hmask, jnp.concatenate([do_r, do_r], axis=0), jnp.zeros((), BF16))
            vw = v_ref[pl.ds(k0, KWIN), :]
            dp = _dot_nt(do2, vw)
            ds = p * (dp - jnp.sum(dp * p, axis=-1, keepdims=True))
            p16 = p.astype(BF16)
            ds16 = ds.astype(BF16)
            dv_acc[pl.ds(k0, KWIN), :] += _dot_tn(p16, do2)
            dk_acc[pl.ds(k0, KWIN), :] += _dot_tn(ds16, q2) * (DH ** -0.5)
            dq2 = _dot(ds16, kw) * (DH ** -0.5)
            dqkv_ref[0, pl.ds(q0, GRID_W), :] = jnp.where(first_head, dq2[:GRID_W], dq2[GRID_W:]).astype(BF16)
            gb_ref[0, oi] += ds
            return carry

        lax.fori_loop(0, ROWS, row, 0, unroll=4)
        dqkv_ref[1] = dk_acc[...].astype(BF16)
        dqkv_ref[2] = dv_acc[...].astype(BF16)

    col = lambda off: pl.BlockSpec((T, 2 * DH), lambda hp: (0, hp + off))
    return pl.pallas_call(
        body, name="att_bwd",
        out_shape=(jax.ShapeDtypeStruct((3, T, D_ATT), BF16),
                   jax.ShapeDtypeStruct((N_HEADS // 2, WIN_H, 2 * GRID_W, KWIN), F32)),
        grid=(N_HEADS // 2,),
        in_specs=[col(0), col(4), col(8), col(0),
                  pl.BlockSpec((WIN_H, 1, 2 * GRID_W, KWIN), lambda hp: (0, hp, 0, 0)),
                  pl.BlockSpec((2 * GRID_W, KWIN), lambda hp: (0, 0)),
                  pl.BlockSpec((2 * DH, 2 * DH), lambda hp: (0, 0))],
        out_specs=(pl.BlockSpec((3, T, 2 * DH), lambda hp: (0, 0, hp)),
                   pl.BlockSpec((1, WIN_H, 2 * GRID_W, KWIN), lambda hp: (hp, 0, 0, 0))),
        scratch_shapes=[pltpu.VMEM((T, 2 * DH), F32), pltpu.VMEM((T, 2 * DH), F32)],
        compiler_params=_params(dimension_semantics=("parallel",)),
    )(qkv, qkv, qkv, datt, bias, jnp.asarray(valid_np) + after, jnp.asarray(hmask_np))


def _conv_taps(up):
    return (_shift_rows(up, 2), _shift_rows(up, 1), up, _shift_rows(up, -1))


def _pair_block_diag(w_pair, dup, same_half):
    return jnp.where(same_half, _dot(w_pair.astype(BF16), dup), 0.0).astype(BF16)


def _gates(u, u16, wa, ba, wi, bi, lam):
    r = _sigmoid(_dot(u16, wa) + ba)
    ig = _sigmoid(_dot(u16, wi) + bi)
    sp = _softplus(-lam)
    log_a = (-LRU_C) * r * sp
    a = jnp.exp(log_a)
    mult = jnp.sqrt(jnp.maximum(_one_minus_square(log_a, a), 0.0))
    return r, ig, sp, a, mult


SCAN_BLOCKS = 2


def _scans(jobs):
    c = jobs[0][0].shape[1]
    nblk = T // 8
    rows = lax.broadcasted_iota(jnp.int32, (8, c), 0)

    def block(a, b, reverse):
        for s in (1, 2, 4):
            if reverse:
                keep = rows < 8 - s
                a_s = jnp.where(keep, pltpu.roll(a, 8 - s, 0), 1.0)
                b_s = jnp.where(keep, pltpu.roll(b, 8 - s, 0), 0.0)
            else:
                keep = rows >= s
                a_s = jnp.where(keep, pltpu.roll(a, s, 0), 1.0)
                b_s = jnp.where(keep, pltpu.roll(b, s, 0), 0.0)
            b = a * b_s + b
            a = a * a_s
        return a, b

    def step(i, carry):
        out = []
        for (a_ref, b_ref, h_ref, reverse), h_prev in zip(jobs, carry):
            for u in range(SCAN_BLOCKS):
                blk = i * SCAN_BLOCKS + u
                if reverse:
                    blk = nblk - 1 - blk
                t0 = pl.multiple_of(blk * 8, 8)
                a, b = block(a_ref[pl.ds(t0, 8), :], b_ref[pl.ds(t0, 8), :], reverse)
                h = a * h_prev + b
                h_ref[pl.ds(t0, 8), :] = h
                h_prev = jnp.broadcast_to(h[0:1] if reverse else h[7:8], (8, c))
            out.append(h_prev)
        return tuple(out)

    lax.fori_loop(0, nblk // SCAN_BLOCKS, step, tuple(jnp.zeros((8, c), F32) for _ in jobs))


def _rec_specs():
    tok = lambda off: pl.BlockSpec((T, CG), lambda g: (0, g + off))
    per_ch = lambda rows: pl.BlockSpec((rows, CG), lambda g: (0, g))
    wspec = pl.BlockSpec((2, 1, CG, REC_BLOCK), lambda g: (0, g, 0, 0))
    const = lambda shape: pl.BlockSpec(shape, lambda g: (0, 0))
    return tok, per_ch, wspec, const


def _rec_fwd(uy, conv_w, conv_b, w_a, b_a, w_i, b_i, lam):
    tok, per_ch, wspec, const = _rec_specs()

    def body(up_ref, yb_ref, cw_ref, cb_ref, wa_ref, ba_ref, wi_ref, bi_ref, lam_ref, dup_ref, half_ref,
             hf_ref, hb_ref, yrec_ref, a_f, bx_f, a_b, bx_b):
        dup = dup_ref[...]
        same_half = half_ref[...] > 0.5
        taps = _conv_taps(up_ref[...])
        u = cb_ref[...]
        for j in range(4):
            u = u + taps[j] * cw_ref[j:j + 1, :]
        u16 = u.astype(BF16)
        for d, (a_s, bx_s) in enumerate(((a_f, bx_f), (a_b, bx_b))):
            wa = _pair_block_diag(wa_ref[d, 0], dup, same_half)
            wi = _pair_block_diag(wi_ref[d, 0], dup, same_half)
            _, ig, _, a, mult = _gates(u, u16, wa, ba_ref[d:d + 1, :], wi, bi_ref[d:d + 1, :],
                                       lam_ref[d:d + 1, :])
            a_s[...] = a
            bx_s[...] = mult * (ig * u)
        _scans([(a_f, bx_f, hf_ref, False), (a_b, bx_b, hb_ref, True)])
        gelu, _ = _gelu_and_grad(yb_ref[...])
        yrec_ref[...] = ((hf_ref[...] + hb_ref[...]) * gelu).astype(BF16)

    return pl.pallas_call(
        body, name="rec_fwd",
        out_shape=(jax.ShapeDtypeStruct((T, D_REC), F32), jax.ShapeDtypeStruct((T, D_REC), F32),
                   jax.ShapeDtypeStruct((T, D_REC), BF16)),
        grid=(N_CG,),
        in_specs=[tok(0), tok(N_CG), per_ch(4), per_ch(1), wspec, per_ch(2), wspec, per_ch(2), per_ch(2),
                  const((REC_BLOCK, CG)), const((CG, CG))],
        out_specs=(tok(0), tok(0), tok(0)),
        scratch_shapes=[pltpu.VMEM((T, CG), F32)] * 4,
        compiler_params=_params(dimension_semantics=("parallel",)),
    )(uy, uy, conv_w, conv_b, w_a, b_a, w_i, b_i, lam,
      jnp.asarray(_dup_table(), BF16), jnp.asarray(_pair_mask()))


def _rec_bwd(uy, hf, hb, dyrec, conv_w, conv_b, w_a, b_a, w_i, b_i, lam):
    tok, per_ch, wspec, const = _rec_specs()

    def body(up_ref, yb_ref, hf_ref, hb_ref, dy_ref, cw_ref, cb_ref, wa_ref, ba_ref, wi_ref, bi_ref,
             lam_ref, dup_ref, dupt_ref, half_ref,
             duy_ref, dcw_ref, dcb_ref, dwa_ref, dba_ref, dwi_ref, dbi_ref, dlam_ref,
             a_s0, a_s1, dh_s, g_s0, g_s1):
        dup = dup_ref[...]
        dup_t = dupt_ref[...]
        same_half = half_ref[...] > 0.5
        taps = _conv_taps(up_ref[...])
        u = cb_ref[...]
        for j in range(4):
            u = u + taps[j] * cw_ref[j:j + 1, :]
        u16 = u.astype(BF16)
        gelu, dgelu = _gelu_and_grad(yb_ref[...])
        dy = dy_ref[...]
        duy_ref[1] = (dy * (hf_ref[...] + hb_ref[...]) * dgelu).astype(BF16)
        dh_s[...] = dy * gelu
        gate_values = []
        for d, a_s in enumerate((a_s0, a_s1)):
            wa = _pair_block_diag(wa_ref[d, 0], dup, same_half)
            wi = _pair_block_diag(wi_ref[d, 0], dup, same_half)
            lam_d = lam_ref[d:d + 1, :]
            r, ig, sp, a, mult = _gates(u, u16, wa, ba_ref[d:d + 1, :], wi, bi_ref[d:d + 1, :], lam_d)
            a_s[...] = _shift_rows(a, 1 if d == 1 else -1)
            gate_values.append((wa, wi, lam_d, r, ig, sp, a, mult))
        _scans([(a_s0, dh_s, g_s0, True), (a_s1, dh_s, g_s1, False)])
        du = jnp.zeros((T, CG), F32)
        for d, g_s in enumerate((g_s0, g_s1)):
            reverse = d == 1
            wa, wi, lam_d, r, ig, sp, a, mult = gate_values[d]
            g = g_s[...]
            h_prev = _shift_rows(hb_ref[...], -1) if reverse else _shift_rows(hf_ref[...], 1)
            da = g * h_prev
            dmult = g * (ig * u)
            dig = g * mult * u
            du = du + g * mult * ig
            dmult_dlog = jnp.where(mult > 0.0, -(a * a) / mult, 0.0)
            dlog_a = da * a + dmult * dmult_dlog
            dr = dlog_a * ((-LRU_C) * sp)
            dsp = jnp.sum(dlog_a * ((-LRU_C) * r), axis=0, keepdims=True)
            dlam_ref[d:d + 1, :] = dsp * (-_sigmoid(-lam_d))
            dga = dr * r * (1.0 - r)
            dgi = dig * ig * (1.0 - ig)
            dga16 = dga.astype(BF16)
            dgi16 = dgi.astype(BF16)
            du = du + _dot_nt(dga16, wa) + _dot_nt(dgi16, wi)
            dwa_ref[d, 0] = _dot_exact(jnp.where(same_half, _dot_tn(u16, dga16), 0.0), dup_t)
            dwi_ref[d, 0] = _dot_exact(jnp.where(same_half, _dot_tn(u16, dgi16), 0.0), dup_t)
            dba_ref[d:d + 1, :] = jnp.sum(dga, axis=0, keepdims=True)
            dbi_ref[d:d + 1, :] = jnp.sum(dgi, axis=0, keepdims=True)
        dcb_ref[...] = jnp.sum(du, axis=0, keepdims=True)
        for j in range(4):
            dcw_ref[j:j + 1, :] = jnp.sum(du * taps[j], axis=0, keepdims=True)
        dup_in = (_shift_rows(du, -2) * cw_ref[0:1, :] + _shift_rows(du, -1) * cw_ref[1:2, :]
                  + du * cw_ref[2:3, :] + _shift_rows(du, 1) * cw_ref[3:4, :])
        duy_ref[0] = dup_in.astype(BF16)

    wshape = jax.ShapeDtypeStruct((2, N_CG, CG, REC_BLOCK), F32)
    vec = lambda rows: jax.ShapeDtypeStruct((rows, D_REC), F32)
    dup_np = _dup_table()
    return pl.pallas_call(
        body, name="rec_bwd",
        out_shape=(jax.ShapeDtypeStruct((2, T, D_REC), BF16),
                   vec(4), vec(1), wshape, vec(2), wshape, vec(2), vec(2)),
        grid=(N_CG,),
        in_specs=[tok(0), tok(N_CG), tok(0), tok(0), tok(0),
                  per_ch(4), per_ch(1), wspec, per_ch(2), wspec, per_ch(2), per_ch(2),
                  const((REC_BLOCK, CG)), const((CG, REC_BLOCK)), const((CG, CG))],
        out_specs=(pl.BlockSpec((2, T, CG), lambda g: (0, 0, g)),
                   per_ch(4), per_ch(1), wspec, per_ch(2), wspec, per_ch(2), per_ch(2)),
        scratch_shapes=[pltpu.VMEM((T, CG), F32)] * 5,
        compiler_params=_params(dimension_semantics=("parallel",)),
    )(uy, uy, hf, hb, dyrec, conv_w, conv_b, w_a, b_a, w_i, b_i, lam,
      jnp.asarray(dup_np, BF16), jnp.asarray(dup_np.T.copy()), jnp.asarray(_pair_mask()))


TM_MIX = 256


def _mix_specs():
    tok = lambda width, blk=0: pl.BlockSpec((TM_MIX, width), lambda i: (i, blk))
    full = lambda shape: pl.BlockSpec(shape, lambda i: (0, 0))
    return tok, full


def _mix_fwd(x, att, yrec, gg, w_att_o_t, w_rec_o, w_out):
    tok, full = _mix_specs()

    def body(x_ref, att_ref, yr_ref, ga_ref, gr_ref, wao_ref, wro_ref, wo_ref, x1_ref, mixed_ref):
        y_att = _dot_nt(att_ref[...], wao_ref[...])
        y_rec = _dot(yr_ref[...], wro_ref[...])
        mixed = (_sigmoid(ga_ref[...]) * y_att + _sigmoid(gr_ref[...]) * y_rec).astype(BF16)
        mixed_ref[...] = mixed
        x1_ref[...] = x_ref[...] + _dot(mixed, wo_ref[...])

    return pl.pallas_call(
        body, name="mix_fwd",
        out_shape=(jax.ShapeDtypeStruct((T, D), F32), jax.ShapeDtypeStruct((T, D), BF16)),
        grid=(T // TM_MIX,),
        in_specs=[tok(D), tok(D_ATT), tok(D_REC), tok(D, 0), tok(D, 1),
                  full((D, D_ATT)), full((D_REC, D)), full((D, D))],
        out_specs=(tok(D), tok(D)),
        compiler_params=_params(dimension_semantics=("parallel",)),
    )(x, att, yrec, gg, gg, w_att_o_t, w_rec_o, w_out)


def _mix_bwd(dx1, att, yrec, gg, w_att_o_t, w_rec_o, w_out):
    tok, full = _mix_specs()

    def body(dx_ref, att_ref, yr_ref, ga_ref, gr_ref, wao_ref, wro_ref, wo_ref,
             dgg_ref, dya_ref, dyr_ref, datt_ref, dyrp_ref):
        dmixed = _dot_nt(dx_ref[...].astype(BF16), wo_ref[...])
        y_att = _dot_nt(att_ref[...], wao_ref[...])
        y_rec = _dot(yr_ref[...], wro_ref[...])
        sa = _sigmoid(ga_ref[...])
        sr = _sigmoid(gr_ref[...])
        dgg_ref[0] = (dmixed * y_att * sa * (1.0 - sa)).astype(BF16)
        dgg_ref[1] = (dmixed * y_rec * sr * (1.0 - sr)).astype(BF16)
        dya = (dmixed * sa).astype(BF16)
        dyr = (dmixed * sr).astype(BF16)
        dya_ref[...] = dya
        dyr_ref[...] = dyr
        datt_ref[...] = _dot(dya, wao_ref[...]).astype(BF16)
        dyrp_ref[...] = _dot_nt(dyr, wro_ref[...])

    return pl.pallas_call(
        body, name="mix_bwd",
        out_shape=(jax.ShapeDtypeStruct((2, T, D), BF16),
                   jax.ShapeDtypeStruct((T, D), BF16), jax.ShapeDtypeStruct((T, D), BF16),
                   jax.ShapeDtypeStruct((T, D_ATT), BF16), jax.ShapeDtypeStruct((T, D_REC), F32)),
        grid=(T // TM_MIX,),
        in_specs=[tok(D), tok(D_ATT), tok(D_REC), tok(D, 0), tok(D, 1),
                  full((D, D_ATT)), full((D_REC, D)), full((D, D))],
        out_specs=(pl.BlockSpec((2, TM_MIX, D), lambda i: (0, i, 0)),
                   tok(D), tok(D), tok(D_ATT), tok(D_REC)),
        compiler_params=_params(dimension_semantics=("parallel",)),
    )(dx1, att, yrec, gg, gg, w_att_o_t, w_rec_o, w_out)


TM_FFN = 256
FF_CHUNK = 1024


def _ffn_loss(x1, target, g2, gf, w_ff1_t, w_ff2):
    n_chunks = D_FF // FF_CHUNK

    def body(x1_ref, tg_ref, g2_ref, gf_ref, w1_hbm, w2_hbm,
             loss_ref, dx1_ref, h2_ref, act_ref, dpre_ref, dx2_ref, dg2_ref, dgf_ref,
             w1, w2, relu_s):
        i = pl.program_id(0)

        @pl.when(i == 0)
        def _():
            pltpu.sync_copy(w1_hbm, w1)
            pltpu.sync_copy(w2_hbm, w2)
            loss_ref[...] = jnp.zeros_like(loss_ref)
            dg2_ref[...] = jnp.zeros_like(dg2_ref)
            dgf_ref[...] = jnp.zeros_like(dgf_ref)

        x1v = x1_ref[...]
        r2 = lax.rsqrt(jnp.mean(x1v * x1v, axis=-1, keepdims=True) + EPS)
        xh2 = x1v * r2
        h2 = (xh2 * g2_ref[...]).astype(BF16)
        h2_ref[...] = h2
        x2 = x1v
        for c in range(n_chunks):
            ff = slice(c * FF_CHUNK, (c + 1) * FF_CHUNK)
            rl = jnp.maximum(_dot_nt(h2, w1[ff, :]), 0.0)
            relu_s[:, ff] = rl
            act = (rl * rl).astype(BF16)
            act_ref[:, ff] = act
            x2 = x2 + _dot(act, w2[ff, :])
        r3 = lax.rsqrt(jnp.mean(x2 * x2, axis=-1, keepdims=True) + EPS)
        xh3 = x2 * r3
        err = xh3 * gf_ref[...] - tg_ref[...]
        loss_ref[...] += 0.5 * jnp.sum(jnp.mean(err * err, axis=-1, keepdims=True))
        dy = err * (1.0 / D)
        dgf_ref[...] += jnp.sum(dy * xh3, axis=0, keepdims=True)
        dx2 = _rms_bwd(dy, xh3, r3, gf_ref[...])
        dx2_16 = dx2.astype(BF16)
        dx2_ref[...] = dx2_16
        dh2 = jnp.zeros((TM_FFN, D), F32)
        for c in range(n_chunks):
            ff = slice(c * FF_CHUNK, (c + 1) * FF_CHUNK)
            dpre = (_dot_nt(dx2_16, w2[ff, :]) * (2.0 * relu_s[:, ff])).astype(BF16)
            dpre_ref[:, ff] = dpre
            dh2 = dh2 + _dot(dpre, w1[ff, :])
        dg2_ref[...] += jnp.sum(dh2 * xh2, axis=0, keepdims=True)
        dx1_ref[...] = dx2 + _rms_bwd(dh2, xh2, r2, g2_ref[...])

    tok = lambda width: pl.BlockSpec((TM_FFN, width), lambda i: (i, 0))
    vec = pl.BlockSpec((1, D), lambda i: (0, 0))
    hbm = pl.BlockSpec(memory_space=pl.ANY)
    return pl.pallas_call(
        body, name="ffn_loss",
        out_shape=(jax.ShapeDtypeStruct((8, 128), F32), jax.ShapeDtypeStruct((T, D), F32),
                   jax.ShapeDtypeStruct((T, D), BF16), jax.ShapeDtypeStruct((T, D_FF), BF16),
                   jax.ShapeDtypeStruct((T, D_FF), BF16), jax.ShapeDtypeStruct((T, D), BF16),
                   jax.ShapeDtypeStruct((1, D), F32), jax.ShapeDtypeStruct((1, D), F32)),
        grid=(T // TM_FFN,),
        in_specs=[tok(D), tok(D), vec, vec, hbm, hbm],
        out_specs=(pl.BlockSpec((8, 128), lambda i: (0, 0)), tok(D), tok(D), tok(D_FF), tok(D_FF), tok(D),
                   vec, vec),
        scratch_shapes=[pltpu.VMEM((D_FF, D), BF16), pltpu.VMEM((D_FF, D), BF16),
                        pltpu.VMEM((TM_FFN, D_FF), F32)],
        compiler_params=_params(dimension_semantics=("arbitrary",)),
    )(x1, target, g2, gf, w_ff1_t, w_ff2)


def _local_step(x, target, p, late_weights, reduce_early):
    bias = _rpb_expand(p["rpb"])
    pairs = lambda w: w.reshape(2, N_CG, CG, REC_BLOCK)
    w_a, w_i = pairs(p["w_rg_a"]), pairs(p["w_rg_i"])
    rec_params = (p["conv_w"], p["conv_b"], w_a, p["b_rg_a"], w_i, p["b_rg_i"], p["lru_lambda"])

    qkv, uy, gg, h = _in_proj(x, p["ln1_g"], p["w_in_t"], p["b_in"])
    att = _att_fwd(qkv, bias)
    hf, hb, yrec = _rec_fwd(uy, *rec_params)
    p = {**p, **late_weights(yrec)}
    x1, mixed = _mix_fwd(x, att, yrec, gg, p["w_att_o_t"], p["w_rec_o"], p["w_out"])
    loss8, dx1, h2, act, dpre, dx2, g_ln2, g_lnf = _ffn_loss(
        x1, target, p["ln2_g"], p["lnf_g"], p["w_ff1_t"], p["w_ff2"])

    dgg, dya, dyr, datt, dyrp = _mix_bwd(dx1, att, yrec, gg, p["w_att_o_t"], p["w_rec_o"], p["w_out"])
    duy, g_cw, g_cb, g_wa, g_ba, g_wi, g_bi, g_lam = _rec_bwd(uy, hf, hb, dyrp, *rec_params)
    blocks = lambda g: g.reshape(2, N_REC_BLOCKS, REC_BLOCK, REC_BLOCK)
    grads = {
        "w_att_o_t": _matmul(dya, att, "tn", BF16, "g_w_att_o"),
        "conv_w": g_cw, "conv_b": g_cb, "w_rg_a": blocks(g_wa), "b_rg_a": g_ba,
        "w_rg_i": blocks(g_wi), "b_rg_i": g_bi, "lru_lambda": g_lam,
        "w_rec_o": _matmul(yrec, dyr, "tn", BF16, "g_w_rec_o"),
        "w_out": _matmul(mixed, dx1, "tn", BF16, "g_w_out"),
        "ln2_g": g_ln2,
        "w_ff1_t": _matmul(dpre, h2, "tn", BF16, "g_w_ff1"),
        "w_ff2": _matmul(act, dx2, "tn", BF16, "g_w_ff2"),
        "lnf_g": g_lnf,
    }
    after = reduce_early(grads)
    dqkv, gbias = _att_bwd(qkv, bias, datt, after)
    dz = (dqkv, duy, dgg)
    grad_x, g_ln1 = _dh_norm1_bwd(dz, p["w_in_t"], x, p["ln1_g"], dx1)
    g_w_in_t, g_b_in = _grad_w_in(dz, h)
    grads.update(ln1_g=g_ln1, w_in_t=g_w_in_t, b_in=g_b_in, rpb=_rpb_reduce(gbias))
    return loss8[0:1, 0:1], grad_x, grads


MESH_ID = pl.DeviceIdType.MESH
ANY = pl.BlockSpec(memory_space=pl.ANY)

CHAN_BLOCK_ROWS = 32
SECTIONS = (("w_in_t", 704, D), ("w_rec_o", 128, D), ("w_out", 128, D), ("w_ff1_t", 512, D),
            ("w_ff2", 512, D), ("chan", CHAN_BLOCK_ROWS, D), ("w_att_o_t", 128, D_ATT))
N_SEC = len(SECTIONS)
N_CHAN_ROWS = 10
CHAN = (("conv_w", 4), ("b_rg_a", 2), ("b_rg_i", 2), ("lru_lambda", 2))


def _position():
    return lax.axis_index("x"), lax.axis_index("y"), lax.axis_index("c")


def _other_chips(x, y):
    return [(1 - x, y), (x, 1 - y), (1 - x, 1 - y)]


def _block_of(ref, dev, rows):
    return ref.at[pl.ds(pl.multiple_of(dev * rows, 16), rows)]


def _all_gather(shards, name):
    ns = len(shards)

    def body(*refs):
        x_refs, out_refs = refs[:ns], refs[ns:2 * ns]
        send_sems, recv_sems, local_sems = refs[2 * ns:]
        x, y, c = _position()
        me, sibling = (x, y, c), (x, y, 1 - c)
        chips = _other_chips(x, y)

        def rows(s, px, py, pc):
            return _block_of(out_refs[s], 4 * px + 2 * py + pc, shards[s].shape[0])

        def copy(k, s, block, to, from_shard=False):
            return pltpu.make_async_remote_copy(
                src_ref=x_refs[s] if from_shard else rows(s, *block), dst_ref=rows(s, *block),
                send_sem=send_sems.at[k * ns + s], recv_sem=recv_sems.at[k * ns + s],
                device_id=to, device_id_type=MESH_ID)

        sections = range(ns)
        mine = [pltpu.make_async_copy(x_refs[s], rows(s, *me), local_sems.at[s]) for s in sections]
        first = [copy(0, s, me, sibling, True) for s in sections]
        first += [copy(1 + j, s, me, (*chip, c), True) for j, chip in enumerate(chips) for s in sections]
        for cp in mine + first:
            cp.start()
        passed = []
        for j, chip in enumerate(chips):
            for s in sections:
                copy(1 + j, s, (*chip, c), me).wait_recv()
                passed.append(copy(4 + j, s, (*chip, c), sibling))
                passed[-1].start()
        for s in sections:
            copy(0, s, sibling, me).wait_recv()
        for j, chip in enumerate(chips):
            for s in sections:
                copy(4 + j, s, (*chip, 1 - c), me).wait_recv()
        for cp in first + passed:
            cp.wait_send()
        for cp in mine:
            cp.wait()

    return pl.pallas_call(
        body, name=name,
        out_shape=tuple(jax.ShapeDtypeStruct((N_DEV * s.shape[0], s.shape[1]), s.dtype) for s in shards),
        in_specs=[ANY] * ns, out_specs=(ANY,) * ns,
        scratch_shapes=[pltpu.SemaphoreType.DMA((7 * ns,)), pltpu.SemaphoreType.DMA((7 * ns,)),
                        pltpu.SemaphoreType.DMA((ns,))],
    )(*shards)


HBM = pl.BlockSpec(memory_space=pltpu.HBM)
SEM = pl.BlockSpec(memory_space=pltpu.SEMAPHORE)
EFFECT = pltpu.SideEffectType.DATAFLOW_SIDE_EFFECTING


def _in_hbm(a):
    return pltpu.with_memory_space_constraint(a, pltpu.HBM)


def _first_hop_copies(shards, x_refs, zones, send_sems, recv_sems):
    ns = len(shards)
    x, y, c = _position()
    targets = [(x, y, 1 - c)] + [(cx, cy, c) for cx, cy in _other_chips(x, y)]
    return [pltpu.make_async_remote_copy(
        src_ref=x_refs[s], dst_ref=_block_of(zones[s], 4 * x + 2 * y + c, shards[s].shape[0]),
        send_sem=send_sems.at[k * ns + s], recv_sem=recv_sems.at[k * ns + s],
        device_id=to, device_id_type=MESH_ID)
        for k, to in enumerate(targets) for s in range(ns)]


def _gather_start(shards, name):
    ns = len(shards)
    x, y, c = _position()
    me = 4 * x + 2 * y + c
    zones = [lax.dynamic_update_slice(lax.empty((N_DEV * s.shape[0], s.shape[1]), s.dtype), s,
                                      (me * s.shape[0], 0)) for s in shards]

    def body(*refs):
        for cp in _first_hop_copies(shards, refs[:ns], refs[ns:2 * ns], refs[2 * ns], refs[2 * ns + 1]):
            cp.start()
        refs[-1][...] = jnp.zeros_like(refs[-1])

    out = pl.pallas_call(
        body, name=name,
        out_shape=(pltpu.SemaphoreType.DMA((4 * ns,)), pltpu.SemaphoreType.DMA((4 * ns,)),
                   *[pltpu.HBM(a.shape, a.dtype) for a in (*shards, *zones)],
                   jax.ShapeDtypeStruct((8, LANES), F32)),
        in_specs=[HBM] * (2 * ns),
        out_specs=(SEM, SEM, *[HBM] * (2 * ns), pl.BlockSpec(memory_space=pltpu.VMEM)),
        input_output_aliases={i: 2 + i for i in range(2 * ns)},
        compiler_params=pltpu.CompilerParams(has_side_effects=EFFECT),
    )(*[_in_hbm(a) for a in shards], *[_in_hbm(a) for a in zones])
    return out[0], out[1], out[2:2 + ns], out[2 + ns:2 + 2 * ns], out[-1]


def _gather_wait(send_sems, recv_sems, shards, zones, after, name):
    ns = len(shards)

    def body(*refs):
        for cp in _first_hop_copies(shards, refs[:ns], refs[ns:2 * ns], refs[2 * ns], refs[2 * ns + 1]):
            cp.wait_send()
            cp.wait_recv()

    out = pl.pallas_call(
        body, name=name,
        out_shape=tuple(pltpu.HBM(a.shape, a.dtype) for a in (*shards, *zones)),
        in_specs=[HBM] * (2 * ns) + [SEM, SEM, ANY],
        out_specs=(HBM,) * (2 * ns),
        input_output_aliases={i: i for i in range(2 * ns)},
        compiler_params=pltpu.CompilerParams(has_side_effects=EFFECT),
    )(*shards, *zones, send_sems, recv_sems, after)
    return out[ns:]


def _gather_pass_on(rows, zones, name):
    ns = len(zones)

    def body(*refs):
        in_refs, out_refs = refs[:ns], refs[ns:2 * ns]
        send_sems, recv_sems = refs[2 * ns:]
        x, y, c = _position()
        copies = [pltpu.make_async_remote_copy(
            src_ref=_block_of(in_refs[s], 4 * cx + 2 * cy + c, rows[s]),
            dst_ref=_block_of(out_refs[s], 4 * cx + 2 * cy + c, rows[s]),
            send_sem=send_sems.at[j * ns + s], recv_sem=recv_sems.at[j * ns + s],
            device_id=(x, y, 1 - c), device_id_type=MESH_ID)
            for j, (cx, cy) in enumerate(_other_chips(x, y)) for s in range(ns)]
        for cp in copies:
            cp.start()
        for cp in copies:
            cp.wait_recv()
        for cp in copies:
            cp.wait_send()

    return pl.pallas_call(
        body, name=name,
        out_shape=tuple(jax.ShapeDtypeStruct(z.shape, z.dtype) for z in zones),
        in_specs=[ANY] * ns, out_specs=(ANY,) * ns,
        input_output_aliases={i: i for i in range(ns)},
        scratch_shapes=[pltpu.SemaphoreType.DMA((3 * ns,)), pltpu.SemaphoreType.DMA((3 * ns,))],
    )(*zones)


def _pair_exchange(sections, grads, name):
    ns = len(sections)

    def body(*refs):
        g_refs, land = refs[:ns], refs[ns:2 * ns]
        send_sems, recv_sems = refs[2 * ns:]
        x, y, c = _position()
        copies = [pltpu.make_async_remote_copy(
            src_ref=_block_of(g_refs[s], 2 * k + 1 - c, rows), dst_ref=land[s].at[k],
            send_sem=send_sems.at[k * ns + s], recv_sem=recv_sems.at[k * ns + s],
            device_id=(x, y, 1 - c), device_id_type=MESH_ID)
            for k in range(N_CHIPS) for s, (_, rows, _) in enumerate(sections)]
        for cp in copies:
            cp.start()
        for cp in copies:
            cp.wait_recv()
        for cp in copies:
            cp.wait_send()

    n = N_CHIPS * ns
    return pl.pallas_call(
        body, name=name,
        out_shape=tuple(jax.ShapeDtypeStruct((N_CHIPS, rows, cols), BF16) for _, rows, cols in sections),
        in_specs=[ANY] * ns, out_specs=(ANY,) * ns,
        scratch_shapes=[pltpu.SemaphoreType.DMA((n,)), pltpu.SemaphoreType.DMA((n,))],
    )(*grads)


def _pair_add(sections, grads, got, core, name):
    ns = len(sections)

    def body(core_ref, *refs):
        g_refs, got_refs, p_refs = refs[:ns], refs[ns:2 * ns], refs[2 * ns:]
        for s in range(ns):
            p_refs[s][0] = (g_refs[s][...].astype(F32) + got_refs[s][0].astype(F32)).astype(BF16)

    slot = [pl.BlockSpec((1, rows, cols), lambda k, c: (k, 0, 0)) for _, rows, cols in sections]
    return pl.pallas_call(
        body, name=name,
        out_shape=tuple(jax.ShapeDtypeStruct((N_CHIPS, rows, cols), BF16) for _, rows, cols in sections),
        grid_spec=pltpu.PrefetchScalarGridSpec(
            num_scalar_prefetch=1, grid=(N_CHIPS,),
            in_specs=[pl.BlockSpec((rows, cols), lambda k, c: (2 * k + c[0], 0)) for _, rows, cols in sections]
            + slot,
            out_specs=tuple(slot)),
        compiler_params=_params(dimension_semantics=("parallel",)),
    )(core, *grads, *got)


def _chip_copies(sections, p_refs, land, send_sems, recv_sems):
    ns = len(sections)
    x, y, c = _position()
    return [pltpu.make_async_remote_copy(
        src_ref=p_refs[s].at[2 * cx + cy], dst_ref=land[s].at[j],
        send_sem=send_sems.at[j * ns + s], recv_sem=recv_sems.at[j * ns + s],
        device_id=(cx, cy, c), device_id_type=MESH_ID)
        for j, (cx, cy) in enumerate(_other_chips(x, y)) for s in range(ns)]


def _chip_exchange(sections, parts, name):
    ns = len(sections)

    def body(*refs):
        copies = _chip_copies(sections, refs[:ns], refs[ns:2 * ns], *refs[2 * ns:])
        for cp in copies:
            cp.start()
        for cp in copies:
            cp.wait_recv()
        for cp in copies:
            cp.wait_send()

    n = 3 * ns
    return pl.pallas_call(
        body, name=name,
        out_shape=tuple(jax.ShapeDtypeStruct((3, rows, cols), BF16) for _, rows, cols in sections),
        in_specs=[ANY] * ns, out_specs=(ANY,) * ns,
        scratch_shapes=[pltpu.SemaphoreType.DMA((n,)), pltpu.SemaphoreType.DMA((n,))],
    )(*parts)


def _chip_exchange_start(sections, parts, name):
    ns = len(sections)

    def body(*refs):
        p_refs, land = refs[:ns], refs[ns:2 * ns]
        send_sems, recv_sems = refs[2 * ns], refs[2 * ns + 1]
        token = refs[-1]
        for cp in _chip_copies(sections, p_refs, land, send_sems, recv_sems):
            cp.start()
        token[...] = jnp.zeros_like(token)

    zones = [lax.empty((3, rows, cols), BF16) for _, rows, cols in sections]
    out = pl.pallas_call(
        body, name=name,
        out_shape=(pltpu.SemaphoreType.DMA((3 * ns,)), pltpu.SemaphoreType.DMA((3 * ns,)),
                   *[pltpu.HBM(a.shape, a.dtype) for a in parts], *[pltpu.HBM(a.shape, a.dtype) for a in zones],
                   jax.ShapeDtypeStruct((8, LANES), F32)),
        in_specs=[HBM] * (2 * ns),
        out_specs=(SEM, SEM, *[HBM] * (2 * ns), pl.BlockSpec(memory_space=pltpu.VMEM)),
        input_output_aliases={i: 2 + i for i in range(2 * ns)},
        compiler_params=pltpu.CompilerParams(has_side_effects=EFFECT),
    )(*[_in_hbm(a) for a in parts], *[_in_hbm(a) for a in zones])
    return out[0], out[1], out[2:2 + ns], out[2 + ns:2 + 2 * ns], out[-1]


def _chip_exchange_wait(sections, send_sems, recv_sems, parts, zones, after, name):
    ns = len(sections)

    def body(*refs):
        p_refs, land = refs[:ns], refs[ns:2 * ns]
        for cp in _chip_copies(sections, p_refs, land, refs[2 * ns], refs[2 * ns + 1]):
            cp.wait_send()
            cp.wait_recv()

    out = pl.pallas_call(
        body, name=name,
        out_shape=tuple(pltpu.HBM(a.shape, a.dtype) for a in (*parts, *zones)),
        in_specs=[HBM] * (2 * ns) + [SEM, SEM, ANY],
        out_specs=(HBM,) * (2 * ns),
        input_output_aliases={i: i for i in range(2 * ns)},
        compiler_params=pltpu.CompilerParams(has_side_effects=EFFECT),
    )(*parts, *zones, send_sems, recv_sems, after)
    return out[:ns], out[ns:]


def _grad_finish(parts, far, chip):
    def body(chip_ref, *refs):
        p_refs, b_refs, g_refs = refs[:N_SEC], refs[N_SEC:2 * N_SEC], refs[2 * N_SEC:]
        for s in range(N_SEC):
            g = p_refs[s][0].astype(F32)
            for j in range(3):
                g = g + b_refs[s][j].astype(F32)
            g_refs[s][...] = g

    half = [(rows // 2, cols) for _, rows, cols in SECTIONS]
    return pl.pallas_call(
        body, name="grad_finish",
        out_shape=tuple(jax.ShapeDtypeStruct((rows, cols), F32) for _, rows, cols in SECTIONS),
        grid_spec=pltpu.PrefetchScalarGridSpec(
            num_scalar_prefetch=1, grid=(2,),
            in_specs=[pl.BlockSpec((1, r, c), lambda i, chip: (chip[0], i, 0)) for r, c in half]
            + [pl.BlockSpec((3, r, c), lambda i, chip: (0, i, 0)) for r, c in half],
            out_specs=tuple(pl.BlockSpec((r, c), lambda i, chip: (i, 0)) for r, c in half)),
        compiler_params=_params(dimension_semantics=("parallel",)),
    )(chip, *parts, *far)


def _sum_devices(parts, rows):
    tr = rows // 2

    def body(*refs):
        s = refs[0][...]
        for d in range(1, N_DEV):
            s = s + refs[d][...]
        refs[N_DEV][...] = s

    return pl.pallas_call(
        body, name="small_grad_sum",
        out_shape=jax.ShapeDtypeStruct((rows, LANES), F32),
        grid=(2,),
        in_specs=[pl.BlockSpec((tr, LANES), lambda i, d=d: (2 * d + i, 0)) for d in range(N_DEV)],
        out_specs=pl.BlockSpec((tr, LANES), lambda i: (i, 0)),
        compiler_params=_params(dimension_semantics=("parallel",)),
    )(*([parts] * N_DEV))


def _adamw(w, g, m, v, name):
    rows, cols = w.shape
    tr = rows
    while tr * cols * 4 > (1 << 20) and tr % 16 == 0:
        tr //= 2
    c1 = 1.0 / (1.0 - ADAM_B1 ** ADAM_STEP)
    c2 = 1.0 / (1.0 - ADAM_B2 ** ADAM_STEP)

    def body(w_ref, g_ref, m_ref, v_ref, d_ref, nm_ref, nv_ref):
        gv = g_ref[...]
        nm = ADAM_B1 * m_ref[...] + (1.0 - ADAM_B1) * gv
        nv = ADAM_B2 * v_ref[...] + (1.0 - ADAM_B2) * (gv * gv)
        nm_ref[...] = nm
        nv_ref[...] = nv
        d_ref[...] = (-ADAM_LR) * ((nm * c1) / (jnp.sqrt(nv * c2) + ADAM_EPS) + ADAM_WD * w_ref[...])

    spec = pl.BlockSpec((tr, cols), lambda i: (i, 0))
    shape = jax.ShapeDtypeStruct((rows, cols), F32)
    return pl.pallas_call(
        body, name=name,
        out_shape=(shape, shape, shape),
        grid=(rows // tr,),
        in_specs=[spec] * 4, out_specs=(spec,) * 3,
        compiler_params=_params(dimension_semantics=("parallel",)),
    )(w, g, m, v)


NAMES = ("ln1_g", "w_in", "b_in", "rpb", "w_att_o", "conv_w", "conv_b", "w_rg_a", "b_rg_a", "w_rg_i",
         "b_rg_i", "lru_lambda", "w_rec_o", "w_out", "ln2_g", "w_ff1", "w_ff2", "lnf_g")
TRANSPOSED = {"w_in": "w_in_t", "w_att_o": "w_att_o_t", "w_ff1": "w_ff1_t"}
ROW_SHARDED = ("w_rec_o", "w_out", "w_ff2")
REPLICATED = (("ln1_g", (1, D)), ("b_in", (1, D_IN)), ("rpb", (N_HEADS * N_RPB_R, N_RPB_C)),
              ("conv_b", (1, D_REC)), ("w_rg_a", (2 * N_REC_BLOCKS * REC_BLOCK, REC_BLOCK)),
              ("w_rg_i", (2 * N_REC_BLOCKS * REC_BLOCK, REC_BLOCK)), ("ln2_g", (1, D)), ("lnf_g", (1, D)))
SMALL_ROWS = 2160


def _chan_bits(vectors):
    chan = jnp.concatenate(vectors, axis=0)
    bits = lax.bitcast_convert_type(chan, BF16).reshape(-1)
    return jnp.pad(bits, (0, CHAN_BLOCK_ROWS * D - bits.shape[0])).reshape(CHAN_BLOCK_ROWS, D)


def _chan_from_bits(gathered):
    bits = gathered.reshape(N_DEV, CHAN_BLOCK_ROWS * D)[:, :2 * N_CHAN_ROWS * LANES]
    chan = lax.bitcast_convert_type(bits.reshape(N_DEV, N_CHAN_ROWS, LANES, 2), F32)
    return chan.transpose(1, 0, 2).reshape(N_CHAN_ROWS, D)


def kernel(x, ln1_g, w_in, b_in, rpb, w_att_o, conv_w, conv_b, w_rg_a, b_rg_a, w_rg_i, b_rg_i, lru_lambda, w_rec_o, w_out, ln2_g, w_ff1, w_ff2, lnf_g, loss_target, m_ln1_g, m_w_in, m_b_in, m_rpb, m_w_att_o, m_conv_w, m_conv_b, m_w_rg_a, m_b_rg_a, m_w_rg_i, m_b_rg_i, m_lru_lambda, m_w_rec_o, m_w_out, m_ln2_g, m_w_ff1, m_w_ff2, m_lnf_g, v_ln1_g, v_w_in, v_b_in, v_rpb, v_w_att_o, v_conv_w, v_conv_b, v_w_rg_a, v_b_rg_a, v_w_rg_i, v_b_rg_i, v_lru_lambda, v_w_rec_o, v_w_out, v_ln2_g, v_w_ff1, v_w_ff2, v_lnf_g):
    w = dict(zip(NAMES, (ln1_g, w_in, b_in, rpb, w_att_o, conv_w, conv_b, w_rg_a, b_rg_a, w_rg_i,
                         b_rg_i, lru_lambda, w_rec_o, w_out, ln2_g, w_ff1, w_ff2, lnf_g)))
    m = dict(zip(NAMES, (m_ln1_g, m_w_in, m_b_in, m_rpb, m_w_att_o, m_conv_w, m_conv_b, m_w_rg_a,
                         m_b_rg_a, m_w_rg_i, m_b_rg_i, m_lru_lambda, m_w_rec_o, m_w_out, m_ln2_g,
                         m_w_ff1, m_w_ff2, m_lnf_g)))
    v = dict(zip(NAMES, (v_ln1_g, v_w_in, v_b_in, v_rpb, v_w_att_o, v_conv_w, v_conv_b, v_w_rg_a,
                         v_b_rg_a, v_w_rg_i, v_b_rg_i, v_lru_lambda, v_w_rec_o, v_w_out, v_ln2_g,
                         v_w_ff1, v_w_ff2, v_lnf_g)))
    xi, yi, ci = _position()

    shard = {t: w[n][0].T.astype(BF16) for n, t in TRANSPOSED.items()}
    shard.update({n: w[n][0].astype(BF16) for n in ROW_SHARDED})
    shard["chan"] = _chan_bits([w[n][0] for n, _ in CHAN])
    first, later = ("w_in_t", "chan"), ("w_rec_o", "w_out", "w_ff1_t", "w_ff2", "w_att_o_t")
    p = dict(zip(first, _all_gather([shard[n] for n in first], "weight_all_gather")))
    send_sems, recv_sems, sent, zones, token = _gather_start([shard[n] for n in later], "weight_gather_start")

    def late_weights(after):
        landed = _gather_wait(send_sems, recv_sems, sent, zones, after, "weight_gather_wait")
        return dict(zip(later, _gather_pass_on([shard[n].shape[0] for n in later], landed,
                                               "weight_gather_pass_on")))

    chan = _chan_from_bits(p.pop("chan"))
    r0 = 0
    for n, rows in CHAN:
        p[n] = chan[r0:r0 + rows]
        r0 += rows
    p.update(ln1_g=w["ln1_g"], b_in=w["b_in"] + token[0, 0], rpb=w["rpb"][0], conv_b=w["conv_b"],
             w_rg_a=w["w_rg_a"][0], w_rg_i=w["w_rg_i"][0], ln2_g=w["ln2_g"],
             lnf_g=w["lnf_g"].reshape(1, D))

    core = jnp.reshape(ci, (1,)).astype(jnp.int32)
    chip = jnp.reshape(2 * xi + yi, (1,)).astype(jnp.int32)
    early_sections, late_sections = SECTIONS[1:], SECTIONS[:1]
    in_flight = {}

    def reduce_early(grads):
        chan_g = jnp.concatenate([grads[n] for n, _ in CHAN], axis=0)
        chan_g = chan_g.reshape(N_CHAN_ROWS, N_DEV, LANES).transpose(1, 0, 2).astype(BF16)
        chan_g = jnp.pad(chan_g.reshape(N_DEV, -1), ((0, 0), (0, CHAN_BLOCK_ROWS * D - N_CHAN_ROWS * LANES)))
        grads["chan"] = chan_g.reshape(N_DEV * CHAN_BLOCK_ROWS, D)
        sect = [grads[n] for n, _, _ in early_sections]
        got = _pair_exchange(early_sections, sect, "grad_pair_exchange_early")
        parts = _pair_add(early_sections, sect, got, core, "grad_pair_add_early")
        in_flight["early"] = _chip_exchange_start(early_sections, parts, "grad_chip_exchange_start")
        return in_flight["early"][-1][0, 0]

    loss_part, grad_x, grads = _local_step(x[0], loss_target[0], p, late_weights, reduce_early)
    sect = [grads[n] for n, _, _ in late_sections]
    got = _pair_exchange(late_sections, sect, "grad_pair_exchange_late")
    late_parts = _pair_add(late_sections, sect, got, core, "grad_pair_add_late")
    late_far = _chip_exchange(late_sections, late_parts, "grad_chip_exchange_late")
    send_sems, recv_sems, early_parts, zones, _ = in_flight["early"]
    early_parts, early_far = _chip_exchange_wait(early_sections, send_sems, recv_sems, early_parts, zones,
                                                 late_far[0], "grad_chip_exchange_wait")
    summed = dict(zip((n for n, _, _ in SECTIONS),
                      _grad_finish([*late_parts, *early_parts], [*late_far, *early_far], chip)))


    flat = jnp.concatenate([grads[n].reshape(-1) for n, _ in REPLICATED] + [loss_part.reshape(-1)])
    n_small = flat.shape[0]
    flat = jnp.pad(flat, (0, SMALL_ROWS * LANES - n_small)).reshape(SMALL_ROWS, LANES)
    (small_parts,) = _all_gather([flat], "small_grad_all_gather")
    small = _sum_devices(small_parts, SMALL_ROWS).reshape(-1)
    loss = small[n_small - 1]

    g, delta, new_m, new_v = {}, {}, {}, {}

    def update(n, g2, shape2):
        d2, m2, v2 = _adamw(w[n].reshape(shape2), g2, m[n].reshape(shape2), v[n].reshape(shape2),
                            "adamw_" + n)
        g[n], delta[n], new_m[n], new_v[n] = (a.reshape(w[n].shape) for a in (g2, d2, m2, v2))

    for n in ROW_SHARDED:
        update(n, summed[n], summed[n].shape)
    for n, t in TRANSPOSED.items():
        update(n, summed[t].T, summed[t].shape[::-1])
    chan_back = summed["chan"].reshape(-1)[:N_CHAN_ROWS * LANES].reshape(N_CHAN_ROWS, LANES)
    r0 = 0
    for n, rows in CHAN:
        update(n, chan_back[r0:r0 + rows], (rows, LANES))
        r0 += rows
    o = 0
    for n, shape2 in REPLICATED:
        size = shape2[0] * shape2[1]
        update(n, small[o:o + size].reshape(shape2), shape2)
        o += size

    return (loss, grad_x[None], *[g[n] for n in NAMES], *[delta[n] for n in NAMES],
            *[new_m[n] for n in NAMES], *[new_v[n] for n in NAMES])
```

```python
import math

import numpy as np
import jax
import jax.numpy as jnp
from jax import lax
from jax.experimental import pallas as pl
from jax.experimental.pallas import tpu as pltpu

F32 = jnp.float32
BF16 = jnp.bfloat16

T = 2048
D = 1024
D_ATT = 512
D_REC = 1024
D_FF = 4096
D_IN = 5632
N_HEADS = 8
DH = 64
GRID_W = 64
ROWS = T // GRID_W
WIN_H = 8
WIN_W = 16
KWIN = WIN_H * GRID_W
N_RPB_R = 2 * WIN_H - 1
N_RPB_C = 2 * WIN_W - 1
N_REC_BLOCKS = 16
REC_BLOCK = 64
CG = 128
N_CG = D_REC // CG
LRU_C = 8.0
EPS = 1e-6
N_DEV = 8
N_CHIPS = 4
LANES = 128

ADAM_LR = 0.001
ADAM_B1 = 0.9
ADAM_B2 = 0.999
ADAM_EPS = 1e-08
ADAM_WD = 0.01
ADAM_STEP = 10

MESH_AXES = ("x", "y", "c")
VMEM_LIMIT = 56 * 1024 * 1024

TILE = 512
DZ_ARRAYS = ((0, 3, 1), (3, 4, 2), (7, 4, 2))
N_DZ_TILES = D_IN // TILE


def _params(**kw):
    return pltpu.CompilerParams(vmem_limit_bytes=VMEM_LIMIT, **kw)


def _att_tables():
    rq = np.arange(2 * GRID_W) % GRID_W
    kc = np.arange(KWIN) % GRID_W
    win_start = np.clip(rq - WIN_W // 2, 0, GRID_W - WIN_W)
    valid = (kc[None, :] >= win_start[:, None]) & (kc[None, :] < win_start[:, None] + WIN_W)
    return valid.astype(np.float32), _pair_mask()


def _pair_mask():
    half = np.arange(2 * DH) // DH
    return (half[:, None] == half[None, :]).astype(np.float32)


def _dup_table():
    return np.concatenate([np.eye(REC_BLOCK, dtype=np.float32)] * 2, axis=1)


def _sigmoid(x):
    return 0.5 * jnp.tanh(0.5 * x) + 0.5


def _softplus(x):
    return jnp.maximum(x, 0.0) + jnp.log(1.0 + jnp.exp(-jnp.abs(x)))


def _one_minus_square(log_a, a):
    x = 2.0 * log_a
    series = -x * (1.0 + x * (0.5 + x * (1.0 / 6.0)))
    return jnp.where(x > -0.02, series, 1.0 - a * a)


_GELU_C = math.sqrt(2.0 / math.pi)


def _gelu_and_grad(x):
    x2 = x * x
    inner = _GELU_C * (x + 0.044715 * x * x2)
    t = jnp.tanh(inner)
    g = 0.5 * x * (1.0 + t)
    dg = 0.5 * (1.0 + t) + 0.5 * x * (1.0 - t * t) * _GELU_C * (1.0 + 3.0 * 0.044715 * x2)
    return g, dg


def _dot(a, b):
    return jnp.dot(a, b, preferred_element_type=F32)


def _dot_nt(a, b):
    return lax.dot_general(a, b, (((1,), (1,)), ((), ())), preferred_element_type=F32)


def _dot_tn(a, b):
    return lax.dot_general(a, b, (((0,), (0,)), ((), ())), preferred_element_type=F32)


def _dot_exact(a, b):
    return jnp.dot(a, b, precision=lax.Precision.HIGHEST, preferred_element_type=F32)


def _shift_rows(x, s):
    n = x.shape[0]
    rows = lax.broadcasted_iota(jnp.int32, x.shape, 0)
    y = pltpu.roll(x, s % n, 0)
    if s > 0:
        return jnp.where(rows >= s, y, 0.0)
    return jnp.where(rows < n + s, y, 0.0)


def _rms_bwd(dh, xh, r, g):
    dxh = dh * g
    return r * (dxh - xh * jnp.mean(dxh * xh, axis=-1, keepdims=True))


def _matmul(a, b, mode, out_dtype, name, tm=512, tn=1024, tk=2048):
    if mode == "nn":
        (m, k), (k2, n) = a.shape, b.shape
    elif mode == "nt":
        (m, k), (n, k2) = a.shape, b.shape
    else:
        (k, m), (k2, n) = a.shape, b.shape
    assert k == k2
    tm, tn, tk = min(tm, m), min(tn, n), min(tk, k)
    assert m % tm == 0 and n % tn == 0 and k % tk == 0
    nk = k // tk
    dot = {"nn": _dot, "nt": _dot_nt, "tn": _dot_tn}[mode]

    def body(a_ref, b_ref, o_ref, acc):
        kk = pl.program_id(2)
        part = dot(a_ref[...].astype(BF16), b_ref[...].astype(BF16))
        if nk == 1:
            o_ref[...] = part.astype(out_dtype)
            return

        @pl.when(kk == 0)
        def _():
            acc[...] = part

        @pl.when(kk > 0)
        def _():
            acc[...] += part

        @pl.when(kk == nk - 1)
        def _():
            o_ref[...] = acc[...].astype(out_dtype)

    if mode == "tn":
        a_spec = pl.BlockSpec((tk, tm), lambda i, j, kk: (kk, i))
    else:
        a_spec = pl.BlockSpec((tm, tk), lambda i, j, kk: (i, kk))
    if mode == "nt":
        b_spec = pl.BlockSpec((tn, tk), lambda i, j, kk: (j, kk))
    else:
        b_spec = pl.BlockSpec((tk, tn), lambda i, j, kk: (kk, j))
    return pl.pallas_call(
        body, name=name,
        out_shape=jax.ShapeDtypeStruct((m, n), out_dtype),
        grid=(m // tm, n // tn, nk),
        in_specs=[a_spec, b_spec],
        out_specs=pl.BlockSpec((tm, tn), lambda i, j, kk: (i, j)),
        scratch_shapes=[pltpu.VMEM((tm, tn) if nk > 1 else (8, LANES), F32)],
        compiler_params=_params(dimension_semantics=("parallel", "parallel", "arbitrary")),
    )(a, b)


def _in_proj(x, g1, w_in_t, b_in):
    tm = 1024

    def body(x_ref, g_ref, w_ref, b_ref, qkv_ref, uy_ref, gg_ref, h_ref, h_scr):
        j = pl.program_id(1)

        @pl.when(j == 0)
        def _():
            xv = x_ref[...]
            r = lax.rsqrt(jnp.mean(xv * xv, axis=-1, keepdims=True) + EPS)
            h = ((xv * r) * g_ref[...]).astype(BF16)
            h_scr[...] = h
            h_ref[...] = h

        z = _dot_nt(h_scr[...], w_ref[...]) + b_ref[...]

        @pl.when(j < 3)
        def _():
            qkv_ref[...] = z.astype(BF16)

        @pl.when((j >= 3) & (j < 7))
        def _():
            uy_ref[...] = z

        @pl.when(j >= 7)
        def _():
            gg_ref[...] = z

    return pl.pallas_call(
        body, name="in_proj",
        out_shape=(jax.ShapeDtypeStruct((T, 3 * D_ATT), BF16),
                   jax.ShapeDtypeStruct((T, 2 * D_REC), F32),
                   jax.ShapeDtypeStruct((T, 2 * D), F32),
                   jax.ShapeDtypeStruct((T, D), BF16)),
        grid=(T // tm, N_DZ_TILES),
        in_specs=[pl.BlockSpec((tm, D), lambda i, j: (i, 0)),
                  pl.BlockSpec((1, D), lambda i, j: (0, 0)),
                  pl.BlockSpec((TILE, D), lambda i, j: (j, 0)),
                  pl.BlockSpec((1, TILE), lambda i, j: (0, j))],
        out_specs=(pl.BlockSpec((tm, TILE), lambda i, j: (i, jnp.minimum(j, 2))),
                   pl.BlockSpec((tm, TILE), lambda i, j: (i, jnp.clip(j - 3, 0, 3))),
                   pl.BlockSpec((tm, TILE), lambda i, j: (i, jnp.clip(j - 7, 0, 3))),
                   pl.BlockSpec((tm, D), lambda i, j: (i, 0))),
        scratch_shapes=[pltpu.VMEM((tm, D), BF16)],
        compiler_params=_params(dimension_semantics=("parallel", "arbitrary")),
    )(x, g1, w_in_t, b_in)


def _dz_specs(rows, tile_of, row_of):
    def spec(off, n, per_plane):
        def index(*ids):
            t = jnp.clip(tile_of(*ids) - off, 0, n - 1)
            return (t // per_plane, row_of(*ids), t % per_plane)
        return pl.BlockSpec((1, rows, TILE), index)
    return [spec(off, n, per) for off, n, per in DZ_ARRAYS]


def _dh_norm1_bwd(dz, w_in_t, x, g1, dx1):
    tm = 1024

    def body(*refs):
        seg_refs = refs[:3]
        w_ref, x_ref, g_ref, dx1_ref, gx_ref, dg_ref, acc = refs[3:]
        i, kk = pl.program_id(0), pl.program_id(1)

        @pl.when(kk == 0)
        def _():
            acc[...] = jnp.zeros_like(acc)

        for s, (off, n, _) in enumerate(DZ_ARRAYS):
            @pl.when((kk >= off) & (kk < off + n))
            def _(s=s):
                acc[...] += _dot(seg_refs[s][0], w_ref[...])

        @pl.when((i == 0) & (kk == 0))
        def _():
            dg_ref[...] = jnp.zeros_like(dg_ref)

        @pl.when(kk == N_DZ_TILES - 1)
        def _():
            xv = x_ref[...]
            r = lax.rsqrt(jnp.mean(xv * xv, axis=-1, keepdims=True) + EPS)
            xh = xv * r
            dh = acc[...]
            dg_ref[...] += jnp.sum(dh * xh, axis=0, keepdims=True)
            gx_ref[...] = dx1_ref[...] + _rms_bwd(dh, xh, r, g_ref[...])

    tok = pl.BlockSpec((tm, D), lambda i, j: (i, 0))
    vec = pl.BlockSpec((1, D), lambda i, j: (0, 0))
    return pl.pallas_call(
        body, name="dh_norm1_bwd",
        out_shape=(jax.ShapeDtypeStruct((T, D), F32), jax.ShapeDtypeStruct((1, D), F32)),
        grid=(T // tm, N_DZ_TILES),
        in_specs=_dz_specs(tm, lambda i, j: j, lambda i, j: i)
        + [pl.BlockSpec((TILE, D), lambda i, j: (j, 0)), tok, vec, tok],
        out_specs=(tok, vec),
        scratch_shapes=[pltpu.VMEM((tm, D), F32)],
        compiler_params=_params(dimension_semantics=("arbitrary", "arbitrary")),
    )(*dz, w_in_t, x, g1, dx1)


def _grad_w_in(dz, h):
    def body(*refs):
        seg_refs = refs[:3]
        h_ref, gw_ref, gb_ref = refs[3:]
        j = pl.program_id(0)

        for s, (off, n, _) in enumerate(DZ_ARRAYS):
            @pl.when((j >= off) & (j < off + n))
            def _(s=s):
                a = seg_refs[s][0]
                gw_ref[...] = _dot_tn(a, h_ref[...]).astype(BF16)
                gb_ref[...] = jnp.sum(a.astype(F32), axis=0, keepdims=True)

    return pl.pallas_call(
        body, name="grad_w_in",
        out_shape=(jax.ShapeDtypeStruct((D_IN, D), BF16), jax.ShapeDtypeStruct((1, D_IN), F32)),
        grid=(N_DZ_TILES,),
        in_specs=_dz_specs(T, lambda j: j, lambda j: 0) + [pl.BlockSpec((T, D), lambda j: (0, 0))],
        out_specs=(pl.BlockSpec((TILE, D), lambda j: (j, 0)), pl.BlockSpec((1, TILE), lambda j: (0, j))),
        compiler_params=_params(dimension_semantics=("parallel",)),
    )(*dz, h)


def _rpb_rows(rpb):
    padded = jnp.pad(rpb, ((0, 0), (0, 0), (0, GRID_W - N_RPB_C)))
    rows = [padded[:, WIN_H - 1 - oi: 2 * WIN_H - 1 - oi].reshape(N_HEADS // 2, 2, KWIN)
            for oi in range(WIN_H)]
    return jnp.stack(rows, axis=0)


SKEW = KWIN - (WIN_W - 1)


def _bias_tiles(rows_ref, bias_s):
    for oi in range(WIN_H):
        for hh in range(2):
            row = jnp.broadcast_to(rows_ref[oi, 0, hh:hh + 1, :], (GRID_W, KWIN))
            bias_s[oi, hh * GRID_W:(hh + 1) * GRID_W, :] = pltpu.roll(row, SKEW, 1, stride=1, stride_axis=0)


def _bias_tile_grads(gb_s, flip, out_ref):
    for oi in range(WIN_H):
        for hh in range(2):
            g = _dot_exact(flip, gb_s[oi, hh * GRID_W:(hh + 1) * GRID_W, :])
            back = pltpu.roll(g, KWIN - (GRID_W - WIN_W), 1, stride=1, stride_axis=0)
            out_ref[0, oi, hh:hh + 1, :] = jnp.sum(back, axis=0, keepdims=True)


def _rpb_fold(row_grads):
    g = row_grads.transpose(1, 0, 2, 3).reshape(WIN_H, N_HEADS, WIN_H, GRID_W)
    g = g.transpose(0, 2, 1, 3)

    def body(g_ref, o_ref):
        for dr in range(N_RPB_R):
            terms = [g_ref[oi, i] for oi in range(WIN_H) for i in range(WIN_H) if i - oi + WIN_H - 1 == dr]
            acc = terms[0]
            for term in terms[1:]:
                acc = acc + term
            o_ref[dr] = acc

    out = pl.pallas_call(
        body, name="rpb_fold",
        out_shape=jax.ShapeDtypeStruct((N_RPB_R, N_HEADS, GRID_W), F32),
    )(g)
    return out.transpose(1, 0, 2)[:, :, :N_RPB_C]


def _att_scores(q_ref, k_ref, bias_ref, valid, hmask, r):
    rs = jnp.clip(r - WIN_H // 2, 0, ROWS - WIN_H)
    oi = r - rs
    q0 = pl.multiple_of(r * GRID_W, GRID_W)
    k0 = pl.multiple_of(rs * GRID_W, GRID_W)
    q_r = q_ref[pl.ds(q0, GRID_W), :]
    q2 = jnp.where(hmask, jnp.concatenate([q_r, q_r], axis=0), jnp.zeros((), BF16))
    kw = k_ref[pl.ds(k0, KWIN), :]
    s = _dot_nt(q2, kw) * (DH ** -0.5) + bias_ref[oi]
    s = jnp.where(valid, s, -1e30)
    m = jnp.max(s, axis=-1, keepdims=True)
    p = jnp.exp(s - m)
    p = p / jnp.sum(p, axis=-1, keepdims=True)
    return p, q2, kw, q0, k0, oi


def _att_fwd(qkv, bias_rows):
    valid_np, hmask_np = _att_tables()

    def body(q_ref, k_ref, v_ref, rows_ref, valid_ref, hmask_ref, o_ref, bias_s):
        valid = valid_ref[...] > 0.5
        hmask = hmask_ref[...] > 0.5
        first_head = lax.broadcasted_iota(jnp.int32, (GRID_W, 2 * DH), 1) < DH
        _bias_tiles(rows_ref, bias_s)

        def row(r, carry):
            p, _, _, q0, k0, _ = _att_scores(q_ref, k_ref, bias_s, valid, hmask, r)
            o2 = _dot(p.astype(BF16), v_ref[pl.ds(k0, KWIN), :])
            o_ref[pl.ds(q0, GRID_W), :] = jnp.where(first_head, o2[:GRID_W], o2[GRID_W:]).astype(BF16)
            return carry

        lax.fori_loop(0, ROWS, row, 0, unroll=4)

    col = lambda off: pl.BlockSpec((T, 2 * DH), lambda hp: (0, hp + off))
    return pl.pallas_call(
        body, name="att_fwd",
        out_shape=jax.ShapeDtypeStruct((T, D_ATT), BF16),
        grid=(N_HEADS // 2,),
        in_specs=[col(0), col(4), col(8),
                  pl.BlockSpec((WIN_H, 1, 2, KWIN), lambda hp: (0, hp, 0, 0)),
                  pl.BlockSpec((2 * GRID_W, KWIN), lambda hp: (0, 0)),
                  pl.BlockSpec((2 * DH, 2 * DH), lambda hp: (0, 0))],
        out_specs=pl.BlockSpec((T, 2 * DH), lambda hp: (0, hp)),
        scratch_shapes=[pltpu.VMEM((WIN_H, 2 * GRID_W, KWIN), F32)],
        compiler_params=_params(dimension_semantics=("parallel",)),
    )(qkv, qkv, qkv, bias_rows, jnp.asarray(valid_np), jnp.asarray(hmask_np))


def _att_bwd(qkv, bias_rows, datt, after):
    valid_np, hmask_np = _att_tables()

    def body(q_ref, k_ref, v_ref, do_ref, rows_ref, valid_ref, hmask_ref, flip_ref,
             dqkv_ref, grows_ref, dk_acc, dv_acc, bias_s, gb_s):
        valid = valid_ref[...] > 0.5
        hmask = hmask_ref[...] > 0.5
        first_head = lax.broadcasted_iota(jnp.int32, (GRID_W, 2 * DH), 1) < DH
        dk_acc[...] = jnp.zeros_like(dk_acc)
        dv_acc[...] = jnp.zeros_like(dv_acc)
        gb_s[...] = jnp.zeros_like(gb_s)
        _bias_tiles(rows_ref, bias_s)

        def row(r, carry):
            p, q2, kw, q0, k0, oi = _att_scores(q_ref, k_ref, bias_s, valid, hmask, r)
            do_r = do_ref[pl.ds(q0, GRID_W), :]
            do2 = jnp.where(hmask, jnp.concatenate([do_r, do_r], axis=0), jnp.zeros((), BF16))
            vw = v_ref[pl.ds(k0, KWIN), :]
            dp = _dot_nt(do2, vw)
            ds = p * (dp - jnp.sum(dp * p, axis=-1, keepdims=True))
            p16 = p.astype(BF16)
            ds16 = ds.astype(BF16)
            dv_acc[pl.ds(k0, KWIN), :] += _dot_tn(p16, do2)
            dk_acc[pl.ds(k0, KWIN), :] += _dot_tn(ds16, q2) * (DH ** -0.5)
            dq2 = _dot(ds16, kw) * (DH ** -0.5)
            dqkv_ref[0, pl.ds(q0, GRID_W), :] = jnp.where(first_head, dq2[:GRID_W], dq2[GRID_W:]).astype(BF16)
            gb_s[oi] += ds
            return carry

        lax.fori_loop(0, ROWS, row, 0, unroll=4)
        dqkv_ref[1] = dk_acc[...].astype(BF16)
        dqkv_ref[2] = dv_acc[...].astype(BF16)
        _bias_tile_grads(gb_s, flip_ref[...], grows_ref)

    col = lambda off: pl.BlockSpec((T, 2 * DH), lambda hp: (0, hp + off))
    tiles = pltpu.VMEM((WIN_H, 2 * GRID_W, KWIN), F32)
    return pl.pallas_call(
        body, name="att_bwd",
        out_shape=(jax.ShapeDtypeStruct((3, T, D_ATT), BF16),
                   jax.ShapeDtypeStruct((N_HEADS // 2, WIN_H, 2, KWIN), F32)),
        grid=(N_HEADS // 2,),
        in_specs=[col(0), col(4), col(8), col(0),
                  pl.BlockSpec((WIN_H, 1, 2, KWIN), lambda hp: (0, hp, 0, 0)),
                  pl.BlockSpec((2 * GRID_W, KWIN), lambda hp: (0, 0)),
                  pl.BlockSpec((2 * DH, 2 * DH), lambda hp: (0, 0)),
                  pl.BlockSpec((GRID_W, GRID_W), lambda hp: (0, 0))],
        out_specs=(pl.BlockSpec((3, T, 2 * DH), lambda hp: (0, 0, hp)),
                   pl.BlockSpec((1, WIN_H, 2, KWIN), lambda hp: (hp, 0, 0, 0))),
        scratch_shapes=[pltpu.VMEM((T, 2 * DH), F32), pltpu.VMEM((T, 2 * DH), F32), tiles, tiles],
        compiler_params=_params(dimension_semantics=("parallel",)),
    )(qkv, qkv, qkv, datt, bias_rows, jnp.asarray(valid_np) + after, jnp.asarray(hmask_np),
      jnp.asarray(np.eye(GRID_W, dtype=np.float32)[::-1].copy()))


def _conv_taps(up):
    return (_shift_rows(up, 2), _shift_rows(up, 1), up, _shift_rows(up, -1))


def _pair_block_diag(w_pair, dup, same_half):
    return jnp.where(same_half, _dot(w_pair.astype(BF16), dup), 0.0).astype(BF16)


def _gates(u, u16, wa, ba, wi, bi, lam):
    r = _sigmoid(_dot(u16, wa) + ba)
    ig = _sigmoid(_dot(u16, wi) + bi)
    sp = _softplus(-lam)
    log_a = (-LRU_C) * r * sp
    a = jnp.exp(log_a)
    mult = jnp.sqrt(jnp.maximum(_one_minus_square(log_a, a), 0.0))
    return r, ig, sp, a, mult


SCAN_BLOCKS = 2


def _scans(jobs):
    c = jobs[0][0].shape[1]
    nblk = T // 8
    rows = lax.broadcasted_iota(jnp.int32, (8, c), 0)

    def block(a, b, reverse):
        for s in (1, 2, 4):
            if reverse:
                keep = rows < 8 - s
                a_s = jnp.where(keep, pltpu.roll(a, 8 - s, 0), 1.0)
                b_s = jnp.where(keep, pltpu.roll(b, 8 - s, 0), 0.0)
            else:
                keep = rows >= s
                a_s = jnp.where(keep, pltpu.roll(a, s, 0), 1.0)
                b_s = jnp.where(keep, pltpu.roll(b, s, 0), 0.0)
            b = a * b_s + b
            a = a * a_s
        return a, b

    def step(i, carry):
        out = []
        for (a_ref, b_ref, h_ref, reverse), h_prev in zip(jobs, carry):
            for u in range(SCAN_BLOCKS):
                blk = i * SCAN_BLOCKS + u
                if reverse:
                    blk = nblk - 1 - blk
                t0 = pl.multiple_of(blk * 8, 8)
                a, b = block(a_ref[pl.ds(t0, 8), :], b_ref[pl.ds(t0, 8), :], reverse)
                h = a * h_prev + b
                h_ref[pl.ds(t0, 8), :] = h
                h_prev = jnp.broadcast_to(h[0:1] if reverse else h[7:8], (8, c))
            out.append(h_prev)
        return tuple(out)

    lax.fori_loop(0, nblk // SCAN_BLOCKS, step, tuple(jnp.zeros((8, c), F32) for _ in jobs))


def _rec_specs():
    tok = lambda off: pl.BlockSpec((T, CG), lambda g: (0, g + off))
    per_ch = lambda rows: pl.BlockSpec((rows, CG), lambda g: (0, g))
    wspec = pl.BlockSpec((2, 1, CG, REC_BLOCK), lambda g: (0, g, 0, 0))
    const = lambda shape: pl.BlockSpec(shape, lambda g: (0, 0))
    return tok, per_ch, wspec, const


def _rec_fwd(uy, conv_w, conv_b, w_a, b_a, w_i, b_i, lam):
    tok, per_ch, wspec, const = _rec_specs()

    def body(up_ref, yb_ref, cw_ref, cb_ref, wa_ref, ba_ref, wi_ref, bi_ref, lam_ref, dup_ref, half_ref,
             hf_ref, hb_ref, yrec_ref, a_f, bx_f, a_b, bx_b):
        dup = dup_ref[...]
        same_half = half_ref[...] > 0.5
        taps = _conv_taps(up_ref[...])
        u = cb_ref[...]
        for j in range(4):
            u = u + taps[j] * cw_ref[j:j + 1, :]
        u16 = u.astype(BF16)
        for d, (a_s, bx_s) in enumerate(((a_f, bx_f), (a_b, bx_b))):
            wa = _pair_block_diag(wa_ref[d, 0], dup, same_half)
            wi = _pair_block_diag(wi_ref[d, 0], dup, same_half)
            _, ig, _, a, mult = _gates(u, u16, wa, ba_ref[d:d + 1, :], wi, bi_ref[d:d + 1, :],
                                       lam_ref[d:d + 1, :])
            a_s[...] = a
            bx_s[...] = mult * (ig * u)
        _scans([(a_f, bx_f, hf_ref, False), (a_b, bx_b, hb_ref, True)])
        gelu, _ = _gelu_and_grad(yb_ref[...])
        yrec_ref[...] = ((hf_ref[...] + hb_ref[...]) * gelu).astype(BF16)

    return pl.pallas_call(
        body, name="rec_fwd",
        out_shape=(jax.ShapeDtypeStruct((T, D_REC), F32), jax.ShapeDtypeStruct((T, D_REC), F32),
                   jax.ShapeDtypeStruct((T, D_REC), BF16)),
        grid=(N_CG,),
        in_specs=[tok(0), tok(N_CG), per_ch(4), per_ch(1), wspec, per_ch(2), wspec, per_ch(2), per_ch(2),
                  const((REC_BLOCK, CG)), const((CG, CG))],
        out_specs=(tok(0), tok(0), tok(0)),
        scratch_shapes=[pltpu.VMEM((T, CG), F32)] * 4,
        compiler_params=_params(dimension_semantics=("parallel",)),
    )(uy, uy, conv_w, conv_b, w_a, b_a, w_i, b_i, lam,
      jnp.asarray(_dup_table(), BF16), jnp.asarray(_pair_mask()))


def _rec_bwd(uy, hf, hb, dyrec, conv_w, conv_b, w_a, b_a, w_i, b_i, lam):
    tok, per_ch, wspec, const = _rec_specs()

    def body(up_ref, yb_ref, hf_ref, hb_ref, dy_ref, cw_ref, cb_ref, wa_ref, ba_ref, wi_ref, bi_ref,
             lam_ref, dup_ref, dupt_ref, half_ref,
             duy_ref, dcw_ref, dcb_ref, dwa_ref, dba_ref, dwi_ref, dbi_ref, dlam_ref,
             a_s0, a_s1, dh_s, g_s0, g_s1):
        dup = dup_ref[...]
        dup_t = dupt_ref[...]
        same_half = half_ref[...] > 0.5
        taps = _conv_taps(up_ref[...])
        u = cb_ref[...]
        for j in range(4):
            u = u + taps[j] * cw_ref[j:j + 1, :]
        u16 = u.astype(BF16)
        gelu, dgelu = _gelu_and_grad(yb_ref[...])
        dy = dy_ref[...]
        duy_ref[1] = (dy * (hf_ref[...] + hb_ref[...]) * dgelu).astype(BF16)
        dh_s[...] = dy * gelu
        gate_values = []
        for d, a_s in enumerate((a_s0, a_s1)):
            wa = _pair_block_diag(wa_ref[d, 0], dup, same_half)
            wi = _pair_block_diag(wi_ref[d, 0], dup, same_half)
            lam_d = lam_ref[d:d + 1, :]
            r, ig, sp, a, mult = _gates(u, u16, wa, ba_ref[d:d + 1, :], wi, bi_ref[d:d + 1, :], lam_d)
            a_s[...] = _shift_rows(a, 1 if d == 1 else -1)
            gate_values.append((wa, wi, lam_d, r, ig, sp, a, mult))
        _scans([(a_s0, dh_s, g_s0, True), (a_s1, dh_s, g_s1, False)])
        du = jnp.zeros((T, CG), F32)
        for d, g_s in enumerate((g_s0, g_s1)):
            reverse = d == 1
            wa, wi, lam_d, r, ig, sp, a, mult = gate_values[d]
            g = g_s[...]
            h_prev = _shift_rows(hb_ref[...], -1) if reverse else _shift_rows(hf_ref[...], 1)
            da = g * h_prev
            dmult = g * (ig * u)
            dig = g * mult * u
            du = du + g * mult * ig
            dmult_dlog = jnp.where(mult > 0.0, -(a * a) / mult, 0.0)
            dlog_a = da * a + dmult * dmult_dlog
            dr = dlog_a * ((-LRU_C) * sp)
            dsp = jnp.sum(dlog_a * ((-LRU_C) * r), axis=0, keepdims=True)
            dlam_ref[d:d + 1, :] = dsp * (-_sigmoid(-lam_d))
            dga = dr * r * (1.0 - r)
            dgi = dig * ig * (1.0 - ig)
            dga16 = dga.astype(BF16)
            dgi16 = dgi.astype(BF16)
            du = du + _dot_nt(dga16, wa) + _dot_nt(dgi16, wi)
            dwa_ref[d, 0] = _dot_exact(jnp.where(same_half, _dot_tn(u16, dga16), 0.0), dup_t)
            dwi_ref[d, 0] = _dot_exact(jnp.where(same_half, _dot_tn(u16, dgi16), 0.0), dup_t)
            dba_ref[d:d + 1, :] = jnp.sum(dga, axis=0, keepdims=True)
            dbi_ref[d:d + 1, :] = jnp.sum(dgi, axis=0, keepdims=True)
        dcb_ref[...] = jnp.sum(du, axis=0, keepdims=True)
        for j in range(4):
            dcw_ref[j:j + 1, :] = jnp.sum(du * taps[j], axis=0, keepdims=True)
        dup_in = (_shift_rows(du, -2) * cw_ref[0:1, :] + _shift_rows(du, -1) * cw_ref[1:2, :]
                  + du * cw_ref[2:3, :] + _shift_rows(du, 1) * cw_ref[3:4, :])
        duy_ref[0] = dup_in.astype(BF16)

    wshape = jax.ShapeDtypeStruct((2, N_CG, CG, REC_BLOCK), F32)
    vec = lambda rows: jax.ShapeDtypeStruct((rows, D_REC), F32)
    dup_np = _dup_table()
    return pl.pallas_call(
        body, name="rec_bwd",
        out_shape=(jax.ShapeDtypeStruct((2, T, D_REC), BF16),
                   vec(4), vec(1), wshape, vec(2), wshape, vec(2), vec(2)),
        grid=(N_CG,),
        in_specs=[tok(0), tok(N_CG), tok(0), tok(0), tok(0),
                  per_ch(4), per_ch(1), wspec, per_ch(2), wspec, per_ch(2), per_ch(2),
                  const((REC_BLOCK, CG)), const((CG, REC_BLOCK)), const((CG, CG))],
        out_specs=(pl.BlockSpec((2, T, CG), lambda g: (0, 0, g)),
                   per_ch(4), per_ch(1), wspec, per_ch(2), wspec, per_ch(2), per_ch(2)),
        scratch_shapes=[pltpu.VMEM((T, CG), F32)] * 5,
        compiler_params=_params(dimension_semantics=("parallel",)),
    )(uy, uy, hf, hb, dyrec, conv_w, conv_b, w_a, b_a, w_i, b_i, lam,
      jnp.asarray(dup_np, BF16), jnp.asarray(dup_np.T.copy()), jnp.asarray(_pair_mask()))


TM_MIX = 256


def _mix_specs():
    tok = lambda width, blk=0: pl.BlockSpec((TM_MIX, width), lambda i: (i, blk))
    full = lambda shape: pl.BlockSpec(shape, lambda i: (0, 0))
    return tok, full


def _mix_fwd(x, att, yrec, gg, w_att_o_t, w_rec_o, w_out):
    tok, full = _mix_specs()

    def body(x_ref, att_ref, yr_ref, ga_ref, gr_ref, wao_ref, wro_ref, wo_ref, x1_ref, mixed_ref):
        y_att = _dot_nt(att_ref[...], wao_ref[...])
        y_rec = _dot(yr_ref[...], wro_ref[...])
        mixed = (_sigmoid(ga_ref[...]) * y_att + _sigmoid(gr_ref[...]) * y_rec).astype(BF16)
        mixed_ref[...] = mixed
        x1_ref[...] = x_ref[...] + _dot(mixed, wo_ref[...])

    return pl.pallas_call(
        body, name="mix_fwd",
        out_shape=(jax.ShapeDtypeStruct((T, D), F32), jax.ShapeDtypeStruct((T, D), BF16)),
        grid=(T // TM_MIX,),
        in_specs=[tok(D), tok(D_ATT), tok(D_REC), tok(D, 0), tok(D, 1),
                  full((D, D_ATT)), full((D_REC, D)), full((D, D))],
        out_specs=(tok(D), tok(D)),
        compiler_params=_params(dimension_semantics=("parallel",)),
    )(x, att, yrec, gg, gg, w_att_o_t, w_rec_o, w_out)


def _mix_bwd(dx1, att, yrec, gg, w_att_o_t, w_rec_o, w_out):
    tok, full = _mix_specs()

    def body(dx_ref, att_ref, yr_ref, ga_ref, gr_ref, wao_ref, wro_ref, wo_ref,
             dgg_ref, dya_ref, dyr_ref, datt_ref, dyrp_ref):
        dmixed = _dot_nt(dx_ref[...].astype(BF16), wo_ref[...])
        y_att = _dot_nt(att_ref[...], wao_ref[...])
        y_rec = _dot(yr_ref[...], wro_ref[...])
        sa = _sigmoid(ga_ref[...])
        sr = _sigmoid(gr_ref[...])
        dgg_ref[0] = (dmixed * y_att * sa * (1.0 - sa)).astype(BF16)
        dgg_ref[1] = (dmixed * y_rec * sr * (1.0 - sr)).astype(BF16)
        dya = (dmixed * sa).astype(BF16)
        dyr = (dmixed * sr).astype(BF16)
        dya_ref[...] = dya
        dyr_ref[...] = dyr
        datt_ref[...] = _dot(dya, wao_ref[...]).astype(BF16)
        dyrp_ref[...] = _dot_nt(dyr, wro_ref[...])

    return pl.pallas_call(
        body, name="mix_bwd",
        out_shape=(jax.ShapeDtypeStruct((2, T, D), BF16),
                   jax.ShapeDtypeStruct((T, D), BF16), jax.ShapeDtypeStruct((T, D), BF16),
                   jax.ShapeDtypeStruct((T, D_ATT), BF16), jax.ShapeDtypeStruct((T, D_REC), F32)),
        grid=(T // TM_MIX,),
        in_specs=[tok(D), tok(D_ATT), tok(D_REC), tok(D, 0), tok(D, 1),
                  full((D, D_ATT)), full((D_REC, D)), full((D, D))],
        out_specs=(pl.BlockSpec((2, TM_MIX, D), lambda i: (0, i, 0)),
                   tok(D), tok(D), tok(D_ATT), tok(D_REC)),
        compiler_params=_params(dimension_semantics=("parallel",)),
    )(dx1, att, yrec, gg, gg, w_att_o_t, w_rec_o, w_out)


TM_FFN = 256
FF_CHUNK = 1024


def _ffn_loss(x1, target, g2, gf, w_ff1_t, w_ff2):
    n_chunks = D_FF // FF_CHUNK

    def body(x1_ref, tg_ref, g2_ref, gf_ref, w1_hbm, w2_hbm,
             loss_ref, dx1_ref, h2_ref, act_ref, dpre_ref, dx2_ref, dg2_ref, dgf_ref,
             w1, w2, relu_s):
        i = pl.program_id(0)

        @pl.when(i == 0)
        def _():
            pltpu.sync_copy(w1_hbm, w1)
            pltpu.sync_copy(w2_hbm, w2)
            loss_ref[...] = jnp.zeros_like(loss_ref)
            dg2_ref[...] = jnp.zeros_like(dg2_ref)
            dgf_ref[...] = jnp.zeros_like(dgf_ref)

        x1v = x1_ref[...]
        r2 = lax.rsqrt(jnp.mean(x1v * x1v, axis=-1, keepdims=True) + EPS)
        xh2 = x1v * r2
        h2 = (xh2 * g2_ref[...]).astype(BF16)
        h2_ref[...] = h2
        x2 = x1v
        for c in range(n_chunks):
            ff = slice(c * FF_CHUNK, (c + 1) * FF_CHUNK)
            rl = jnp.maximum(_dot_nt(h2, w1[ff, :]), 0.0)
            relu_s[:, ff] = rl
            act = (rl * rl).astype(BF16)
            act_ref[:, ff] = act
            x2 = x2 + _dot(act, w2[ff, :])
        r3 = lax.rsqrt(jnp.mean(x2 * x2, axis=-1, keepdims=True) + EPS)
        xh3 = x2 * r3
        err = xh3 * gf_ref[...] - tg_ref[...]
        loss_ref[...] += 0.5 * jnp.sum(jnp.mean(err * err, axis=-1, keepdims=True))
        dy = err * (1.0 / D)
        dgf_ref[...] += jnp.sum(dy * xh3, axis=0, keepdims=True)
        dx2 = _rms_bwd(dy, xh3, r3, gf_ref[...])
        dx2_16 = dx2.astype(BF16)
        dx2_ref[...] = dx2_16
        dh2 = jnp.zeros((TM_FFN, D), F32)
        for c in range(n_chunks):
            ff = slice(c * FF_CHUNK, (c + 1) * FF_CHUNK)
            dpre = (_dot_nt(dx2_16, w2[ff, :]) * (2.0 * relu_s[:, ff])).astype(BF16)
            dpre_ref[:, ff] = dpre
            dh2 = dh2 + _dot(dpre, w1[ff, :])
        dg2_ref[...] += jnp.sum(dh2 * xh2, axis=0, keepdims=True)
        dx1_ref[...] = dx2 + _rms_bwd(dh2, xh2, r2, g2_ref[...])

    tok = lambda width: pl.BlockSpec((TM_FFN, width), lambda i: (i, 0))
    vec = pl.BlockSpec((1, D), lambda i: (0, 0))
    hbm = pl.BlockSpec(memory_space=pl.ANY)
    return pl.pallas_call(
        body, name="ffn_loss",
        out_shape=(jax.ShapeDtypeStruct((8, 128), F32), jax.ShapeDtypeStruct((T, D), F32),
                   jax.ShapeDtypeStruct((T, D), BF16), jax.ShapeDtypeStruct((T, D_FF), BF16),
                   jax.ShapeDtypeStruct((T, D_FF), BF16), jax.ShapeDtypeStruct((T, D), BF16),
                   jax.ShapeDtypeStruct((1, D), F32), jax.ShapeDtypeStruct((1, D), F32)),
        grid=(T // TM_FFN,),
        in_specs=[tok(D), tok(D), vec, vec, hbm, hbm],
        out_specs=(pl.BlockSpec((8, 128), lambda i: (0, 0)), tok(D), tok(D), tok(D_FF), tok(D_FF), tok(D),
                   vec, vec),
        scratch_shapes=[pltpu.VMEM((D_FF, D), BF16), pltpu.VMEM((D_FF, D), BF16),
                        pltpu.VMEM((TM_FFN, D_FF), F32)],
        compiler_params=_params(dimension_semantics=("arbitrary",)),
    )(x1, target, g2, gf, w_ff1_t, w_ff2)


def _local_step(x, target, p, late_weights, reduce_early):
    bias = _rpb_rows(p["rpb"])
    pairs = lambda w: w.reshape(2, N_CG, CG, REC_BLOCK)
    w_a, w_i = pairs(p["w_rg_a"]), pairs(p["w_rg_i"])
    rec_params = (p["conv_w"], p["conv_b"], w_a, p["b_rg_a"], w_i, p["b_rg_i"], p["lru_lambda"])

    qkv, uy, gg, h = _in_proj(x, p["ln1_g"], p["w_in_t"], p["b_in"])
    att = _att_fwd(qkv, bias)
    hf, hb, yrec = _rec_fwd(uy, *rec_params)
    p = {**p, **late_weights(yrec)}
    x1, mixed = _mix_fwd(x, att, yrec, gg, p["w_att_o_t"], p["w_rec_o"], p["w_out"])
    loss8, dx1, h2, act, dpre, dx2, g_ln2, g_lnf = _ffn_loss(
        x1, target, p["ln2_g"], p["lnf_g"], p["w_ff1_t"], p["w_ff2"])

    dgg, dya, dyr, datt, dyrp = _mix_bwd(dx1, att, yrec, gg, p["w_att_o_t"], p["w_rec_o"], p["w_out"])
    duy, g_cw, g_cb, g_wa, g_ba, g_wi, g_bi, g_lam = _rec_bwd(uy, hf, hb, dyrp, *rec_params)
    blocks = lambda g: g.reshape(2, N_REC_BLOCKS, REC_BLOCK, REC_BLOCK)
    grads = {
        "w_att_o_t": _matmul(dya, att, "tn", BF16, "g_w_att_o"),
        "conv_w": g_cw, "conv_b": g_cb, "w_rg_a": blocks(g_wa), "b_rg_a": g_ba,
        "w_rg_i": blocks(g_wi), "b_rg_i": g_bi, "lru_lambda": g_lam,
        "w_rec_o": _matmul(yrec, dyr, "tn", BF16, "g_w_rec_o"),
        "w_out": _matmul(mixed, dx1, "tn", BF16, "g_w_out"),
        "ln2_g": g_ln2,
        "w_ff1_t": _matmul(dpre, h2, "tn", BF16, "g_w_ff1"),
        "w_ff2": _matmul(act, dx2, "tn", BF16, "g_w_ff2"),
        "lnf_g": g_lnf,
    }
    after = reduce_early(grads)
    dqkv, gbias = _att_bwd(qkv, bias, datt, after)
    dz = (dqkv, duy, dgg)
    grad_x, g_ln1 = _dh_norm1_bwd(dz, p["w_in_t"], x, p["ln1_g"], dx1)
    g_w_in_t, g_b_in = _grad_w_in(dz, h)
    grads.update(ln1_g=g_ln1, w_in_t=g_w_in_t, b_in=g_b_in, rpb=_rpb_fold(gbias))
    return loss8[0:1, 0:1], grad_x, grads


MESH_ID = pl.DeviceIdType.MESH
ANY = pl.BlockSpec(memory_space=pl.ANY)

CHAN_BLOCK_ROWS = 32
SECTIONS = (("w_in_t", 704, D), ("w_rec_o", 128, D), ("w_out", 128, D), ("w_ff1_t", 512, D),
            ("w_ff2", 512, D), ("chan", CHAN_BLOCK_ROWS, D), ("w_att_o_t", 128, D_ATT))
N_SEC = len(SECTIONS)
N_CHAN_ROWS = 10
CHAN = (("conv_w", 4), ("b_rg_a", 2), ("b_rg_i", 2), ("lru_lambda", 2))


def _position():
    return lax.axis_index("x"), lax.axis_index("y"), lax.axis_index("c")


def _other_chips(x, y):
    return [(1 - x, y), (x, 1 - y), (1 - x, 1 - y)]


def _block_of(ref, dev, rows):
    return ref.at[pl.ds(pl.multiple_of(dev * rows, 16), rows)]


def _all_gather(shards, name):
    ns = len(shards)

    def body(*refs):
        x_refs, out_refs = refs[:ns], refs[ns:2 * ns]
        send_sems, recv_sems, local_sems = refs[2 * ns:]
        x, y, c = _position()
        me, sibling = (x, y, c), (x, y, 1 - c)
        chips = _other_chips(x, y)

        def rows(s, px, py, pc):
            return _block_of(out_refs[s], 4 * px + 2 * py + pc, shards[s].shape[0])

        def copy(k, s, block, to, from_shard=False):
            return pltpu.make_async_remote_copy(
                src_ref=x_refs[s] if from_shard else rows(s, *block), dst_ref=rows(s, *block),
                send_sem=send_sems.at[k * ns + s], recv_sem=recv_sems.at[k * ns + s],
                device_id=to, device_id_type=MESH_ID)

        sections = range(ns)
        mine = [pltpu.make_async_copy(x_refs[s], rows(s, *me), local_sems.at[s]) for s in sections]
        first = [copy(0, s, me, sibling, True) for s in sections]
        first += [copy(1 + j, s, me, (*chip, c), True) for j, chip in enumerate(chips) for s in sections]
        for cp in mine + first:
            cp.start()
        passed = []
        for j, chip in enumerate(chips):
            for s in sections:
                copy(1 + j, s, (*chip, c), me).wait_recv()
                passed.append(copy(4 + j, s, (*chip, c), sibling))
                passed[-1].start()
        for s in sections:
            copy(0, s, sibling, me).wait_recv()
        for j, chip in enumerate(chips):
            for s in sections:
                copy(4 + j, s, (*chip, 1 - c), me).wait_recv()
        for cp in first + passed:
            cp.wait_send()
        for cp in mine:
            cp.wait()

    return pl.pallas_call(
        body, name=name,
        out_shape=tuple(jax.ShapeDtypeStruct((N_DEV * s.shape[0], s.shape[1]), s.dtype) for s in shards),
        in_specs=[ANY] * ns, out_specs=(ANY,) * ns,
        scratch_shapes=[pltpu.SemaphoreType.DMA((7 * ns,)), pltpu.SemaphoreType.DMA((7 * ns,)),
                        pltpu.SemaphoreType.DMA((ns,))],
    )(*shards)


HBM = pl.BlockSpec(memory_space=pltpu.HBM)
SEM = pl.BlockSpec(memory_space=pltpu.SEMAPHORE)
EFFECT = pltpu.SideEffectType.DATAFLOW_SIDE_EFFECTING


def _in_hbm(a):
    return pltpu.with_memory_space_constraint(a, pltpu.HBM)


def _first_hop_copies(shards, x_refs, zones, send_sems, recv_sems):
    ns = len(shards)
    x, y, c = _position()
    targets = [(x, y, 1 - c)] + [(cx, cy, c) for cx, cy in _other_chips(x, y)]
    return [pltpu.make_async_remote_copy(
        src_ref=x_refs[s], dst_ref=_block_of(zones[s], 4 * x + 2 * y + c, shards[s].shape[0]),
        send_sem=send_sems.at[k * ns + s], recv_sem=recv_sems.at[k * ns + s],
        device_id=to, device_id_type=MESH_ID)
        for k, to in enumerate(targets) for s in range(ns)]


def _own_blocks_placed(shards):
    ns = len(shards)
    x, y, c = _position()
    me = jnp.reshape(4 * x + 2 * y + c, (1,)).astype(jnp.int32)

    def body(me_ref, *refs):
        for s in range(ns):
            refs[ns + s][...] = refs[s][...]

    return pl.pallas_call(
        body, name="own_blocks_placed",
        out_shape=tuple(jax.ShapeDtypeStruct((N_DEV * s.shape[0], s.shape[1]), s.dtype) for s in shards),
        grid_spec=pltpu.PrefetchScalarGridSpec(
            num_scalar_prefetch=1, grid=(1,),
            in_specs=[pl.BlockSpec(s.shape, lambda i, me: (0, 0)) for s in shards],
            out_specs=tuple(pl.BlockSpec(s.shape, lambda i, me: (me[0], 0)) for s in shards)),
        compiler_params=_params(dimension_semantics=("arbitrary",)),
    )(me, *shards)


def _gather_start(shards, name):
    ns = len(shards)
    zones = _own_blocks_placed(shards)

    def body(*refs):
        for cp in _first_hop_copies(shards, refs[:ns], refs[ns:2 * ns], refs[2 * ns], refs[2 * ns + 1]):
            cp.start()
        refs[-1][...] = jnp.zeros_like(refs[-1])

    out = pl.pallas_call(
        body, name=name,
        out_shape=(pltpu.SemaphoreType.DMA((4 * ns,)), pltpu.SemaphoreType.DMA((4 * ns,)),
                   *[pltpu.HBM(a.shape, a.dtype) for a in (*shards, *zones)],
                   jax.ShapeDtypeStruct((8, LANES), F32)),
        in_specs=[HBM] * (2 * ns),
        out_specs=(SEM, SEM, *[HBM] * (2 * ns), pl.BlockSpec(memory_space=pltpu.VMEM)),
        input_output_aliases={i: 2 + i for i in range(2 * ns)},
        compiler_params=pltpu.CompilerParams(has_side_effects=EFFECT),
    )(*[_in_hbm(a) for a in shards], *[_in_hbm(a) for a in zones])
    return out[0], out[1], out[2:2 + ns], out[2 + ns:2 + 2 * ns], out[-1]


def _gather_wait(send_sems, recv_sems, shards, zones, after, name):
    ns = len(shards)

    def body(*refs):
        for cp in _first_hop_copies(shards, refs[:ns], refs[ns:2 * ns], refs[2 * ns], refs[2 * ns + 1]):
            cp.wait_send()
            cp.wait_recv()

    out = pl.pallas_call(
        body, name=name,
        out_shape=tuple(pltpu.HBM(a.shape, a.dtype) for a in (*shards, *zones)),
        in_specs=[HBM] * (2 * ns) + [SEM, SEM, ANY],
        out_specs=(HBM,) * (2 * ns),
        input_output_aliases={i: i for i in range(2 * ns)},
        compiler_params=pltpu.CompilerParams(has_side_effects=EFFECT),
    )(*shards, *zones, send_sems, recv_sems, after)
    return out[ns:]


def _gather_pass_on(rows, zones, name):
    ns = len(zones)

    def body(*refs):
        in_refs, out_refs = refs[:ns], refs[ns:2 * ns]
        send_sems, recv_sems = refs[2 * ns:]
        x, y, c = _position()
        copies = [pltpu.make_async_remote_copy(
            src_ref=_block_of(in_refs[s], 4 * cx + 2 * cy + c, rows[s]),
            dst_ref=_block_of(out_refs[s], 4 * cx + 2 * cy + c, rows[s]),
            send_sem=send_sems.at[j * ns + s], recv_sem=recv_sems.at[j * ns + s],
            device_id=(x, y, 1 - c), device_id_type=MESH_ID)
            for j, (cx, cy) in enumerate(_other_chips(x, y)) for s in range(ns)]
        for cp in copies:
            cp.start()
        for cp in copies:
            cp.wait_recv()
        for cp in copies:
            cp.wait_send()

    return pl.pallas_call(
        body, name=name,
        out_shape=tuple(jax.ShapeDtypeStruct(z.shape, z.dtype) for z in zones),
        in_specs=[ANY] * ns, out_specs=(ANY,) * ns,
        input_output_aliases={i: i for i in range(ns)},
        scratch_shapes=[pltpu.SemaphoreType.DMA((3 * ns,)), pltpu.SemaphoreType.DMA((3 * ns,))],
    )(*zones)


def _pair_exchange(sections, grads, name):
    ns = len(sections)

    def body(*refs):
        g_refs, land = refs[:ns], refs[ns:2 * ns]
        send_sems, recv_sems = refs[2 * ns:]
        x, y, c = _position()
        copies = [pltpu.make_async_remote_copy(
            src_ref=_block_of(g_refs[s], 2 * k + 1 - c, rows), dst_ref=land[s].at[k],
            send_sem=send_sems.at[k * ns + s], recv_sem=recv_sems.at[k * ns + s],
            device_id=(x, y, 1 - c), device_id_type=MESH_ID)
            for k in range(N_CHIPS) for s, (_, rows, _) in enumerate(sections)]
        for cp in copies:
            cp.start()
        for cp in copies:
            cp.wait_recv()
        for cp in copies:
            cp.wait_send()

    n = N_CHIPS * ns
    return pl.pallas_call(
        body, name=name,
        out_shape=tuple(jax.ShapeDtypeStruct((N_CHIPS, rows, cols), BF16) for _, rows, cols in sections),
        in_specs=[ANY] * ns, out_specs=(ANY,) * ns,
        scratch_shapes=[pltpu.SemaphoreType.DMA((n,)), pltpu.SemaphoreType.DMA((n,))],
    )(*grads)


def _pair_add(sections, grads, got, core, name):
    ns = len(sections)

    def body(core_ref, *refs):
        g_refs, got_refs, p_refs = refs[:ns], refs[ns:2 * ns], refs[2 * ns:]
        for s in range(ns):
            p_refs[s][0] = (g_refs[s][...].astype(F32) + got_refs[s][0].astype(F32)).astype(BF16)

    slot = [pl.BlockSpec((1, rows, cols), lambda k, c: (k, 0, 0)) for _, rows, cols in sections]
    return pl.pallas_call(
        body, name=name,
        out_shape=tuple(jax.ShapeDtypeStruct((N_CHIPS, rows, cols), BF16) for _, rows, cols in sections),
        grid_spec=pltpu.PrefetchScalarGridSpec(
            num_scalar_prefetch=1, grid=(N_CHIPS,),
            in_specs=[pl.BlockSpec((rows, cols), lambda k, c: (2 * k + c[0], 0)) for _, rows, cols in sections]
            + slot,
            out_specs=tuple(slot)),
        compiler_params=_params(dimension_semantics=("parallel",)),
    )(core, *grads, *got)


def _chip_copies(sections, p_refs, land, send_sems, recv_sems):
    ns = len(sections)
    x, y, c = _position()
    return [pltpu.make_async_remote_copy(
        src_ref=p_refs[s].at[2 * cx + cy], dst_ref=land[s].at[j],
        send_sem=send_sems.at[j * ns + s], recv_sem=recv_sems.at[j * ns + s],
        device_id=(cx, cy, c), device_id_type=MESH_ID)
        for j, (cx, cy) in enumerate(_other_chips(x, y)) for s in range(ns)]


def _chip_exchange(sections, parts, name):
    ns = len(sections)

    def body(*refs):
        copies = _chip_copies(sections, refs[:ns], refs[ns:2 * ns], *refs[2 * ns:])
        for cp in copies:
            cp.start()
        for cp in copies:
            cp.wait_recv()
        for cp in copies:
            cp.wait_send()

    n = 3 * ns
    return pl.pallas_call(
        body, name=name,
        out_shape=tuple(jax.ShapeDtypeStruct((3, rows, cols), BF16) for _, rows, cols in sections),
        in_specs=[ANY] * ns, out_specs=(ANY,) * ns,
        scratch_shapes=[pltpu.SemaphoreType.DMA((n,)), pltpu.SemaphoreType.DMA((n,))],
    )(*parts)


def _chip_exchange_start(sections, parts, name):
    ns = len(sections)

    def body(*refs):
        p_refs, land = refs[:ns], refs[ns:2 * ns]
        send_sems, recv_sems = refs[2 * ns], refs[2 * ns + 1]
        token = refs[-1]
        for cp in _chip_copies(sections, p_refs, land, send_sems, recv_sems):
            cp.start()
        token[...] = jnp.zeros_like(token)

    zones = [lax.empty((3, rows, cols), BF16) for _, rows, cols in sections]
    out = pl.pallas_call(
        body, name=name,
        out_shape=(pltpu.SemaphoreType.DMA((3 * ns,)), pltpu.SemaphoreType.DMA((3 * ns,)),
                   *[pltpu.HBM(a.shape, a.dtype) for a in parts], *[pltpu.HBM(a.shape, a.dtype) for a in zones],
                   jax.ShapeDtypeStruct((8, LANES), F32)),
        in_specs=[HBM] * (2 * ns),
        out_specs=(SEM, SEM, *[HBM] * (2 * ns), pl.BlockSpec(memory_space=pltpu.VMEM)),
        input_output_aliases={i: 2 + i for i in range(2 * ns)},
        compiler_params=pltpu.CompilerParams(has_side_effects=EFFECT),
    )(*[_in_hbm(a) for a in parts], *[_in_hbm(a) for a in zones])
    return out[0], out[1], out[2:2 + ns], out[2 + ns:2 + 2 * ns], out[-1]


def _chip_exchange_wait(sections, send_sems, recv_sems, parts, zones, after, name):
    ns = len(sections)

    def body(*refs):
        p_refs, land = refs[:ns], refs[ns:2 * ns]
        for cp in _chip_copies(sections, p_refs, land, refs[2 * ns], refs[2 * ns + 1]):
            cp.wait_send()
            cp.wait_recv()

    out = pl.pallas_call(
        body, name=name,
        out_shape=tuple(pltpu.HBM(a.shape, a.dtype) for a in (*parts, *zones)),
        in_specs=[HBM] * (2 * ns) + [SEM, SEM, ANY],
        out_specs=(HBM,) * (2 * ns),
        input_output_aliases={i: i for i in range(2 * ns)},
        compiler_params=pltpu.CompilerParams(has_side_effects=EFFECT),
    )(*parts, *zones, send_sems, recv_sems, after)
    return out[:ns], out[ns:]


def _grad_finish(parts, far, chip):
    def body(chip_ref, *refs):
        p_refs, b_refs, g_refs = refs[:N_SEC], refs[N_SEC:2 * N_SEC], refs[2 * N_SEC:]
        for s in range(N_SEC):
            g = p_refs[s][0].astype(F32)
            for j in range(3):
                g = g + b_refs[s][j].astype(F32)
            g_refs[s][...] = g

    half = [(rows // 2, cols) for _, rows, cols in SECTIONS]
    return pl.pallas_call(
        body, name="grad_finish",
        out_shape=tuple(jax.ShapeDtypeStruct((rows, cols), F32) for _, rows, cols in SECTIONS),
        grid_spec=pltpu.PrefetchScalarGridSpec(
            num_scalar_prefetch=1, grid=(2,),
            in_specs=[pl.BlockSpec((1, r, c), lambda i, chip: (chip[0], i, 0)) for r, c in half]
            + [pl.BlockSpec((3, r, c), lambda i, chip: (0, i, 0)) for r, c in half],
            out_specs=tuple(pl.BlockSpec((r, c), lambda i, chip: (i, 0)) for r, c in half)),
        compiler_params=_params(dimension_semantics=("parallel",)),
    )(chip, *parts, *far)


def _sum_devices(parts, rows):
    tr = rows // 2

    def body(*refs):
        s = refs[0][...]
        for d in range(1, N_DEV):
            s = s + refs[d][...]
        refs[N_DEV][...] = s

    return pl.pallas_call(
        body, name="small_grad_sum",
        out_shape=jax.ShapeDtypeStruct((rows, LANES), F32),
        grid=(2,),
        in_specs=[pl.BlockSpec((tr, LANES), lambda i, d=d: (2 * d + i, 0)) for d in range(N_DEV)],
        out_specs=pl.BlockSpec((tr, LANES), lambda i: (i, 0)),
        compiler_params=_params(dimension_semantics=("parallel",)),
    )(*([parts] * N_DEV))


def _adamw(w, g, m, v, name):
    rows, cols = w.shape
    tr = rows
    while tr * cols * 4 > (1 << 20) and tr % 16 == 0:
        tr //= 2
    c1 = 1.0 / (1.0 - ADAM_B1 ** ADAM_STEP)
    c2 = 1.0 / (1.0 - ADAM_B2 ** ADAM_STEP)

    def body(w_ref, g_ref, m_ref, v_ref, d_ref, nm_ref, nv_ref):
        gv = g_ref[...]
        nm = ADAM_B1 * m_ref[...] + (1.0 - ADAM_B1) * gv
        nv = ADAM_B2 * v_ref[...] + (1.0 - ADAM_B2) * (gv * gv)
        nm_ref[...] = nm
        nv_ref[...] = nv
        d_ref[...] = (-ADAM_LR) * ((nm * c1) / (jnp.sqrt(nv * c2) + ADAM_EPS) + ADAM_WD * w_ref[...])

    spec = pl.BlockSpec((tr, cols), lambda i: (i, 0))
    shape = jax.ShapeDtypeStruct((rows, cols), F32)
    return pl.pallas_call(
        body, name=name,
        out_shape=(shape, shape, shape),
        grid=(rows // tr,),
        in_specs=[spec] * 4, out_specs=(spec,) * 3,
        compiler_params=_params(dimension_semantics=("parallel",)),
    )(w, g, m, v)


NAMES = ("ln1_g", "w_in", "b_in", "rpb", "w_att_o", "conv_w", "conv_b", "w_rg_a", "b_rg_a", "w_rg_i",
         "b_rg_i", "lru_lambda", "w_rec_o", "w_out", "ln2_g", "w_ff1", "w_ff2", "lnf_g")
TRANSPOSED = {"w_in": "w_in_t", "w_att_o": "w_att_o_t", "w_ff1": "w_ff1_t"}
ROW_SHARDED = ("w_rec_o", "w_out", "w_ff2")
REPLICATED = (("ln1_g", (1, D)), ("b_in", (1, D_IN)), ("rpb", (N_HEADS * N_RPB_R, N_RPB_C)),
              ("conv_b", (1, D_REC)), ("w_rg_a", (2 * N_REC_BLOCKS * REC_BLOCK, REC_BLOCK)),
              ("w_rg_i", (2 * N_REC_BLOCKS * REC_BLOCK, REC_BLOCK)), ("ln2_g", (1, D)), ("lnf_g", (1, D)))
SMALL_ROWS = 2160


def _chan_bits(vectors):
    chan = jnp.concatenate(vectors, axis=0)
    bits = lax.bitcast_convert_type(chan, BF16).reshape(-1)
    return jnp.pad(bits, (0, CHAN_BLOCK_ROWS * D - bits.shape[0])).reshape(CHAN_BLOCK_ROWS, D)


def _chan_from_bits(gathered):
    bits = gathered.reshape(N_DEV, CHAN_BLOCK_ROWS * D)[:, :2 * N_CHAN_ROWS * LANES]
    chan = lax.bitcast_convert_type(bits.reshape(N_DEV, N_CHAN_ROWS, LANES, 2), F32)
    return chan.transpose(1, 0, 2).reshape(N_CHAN_ROWS, D)


def kernel(x, ln1_g, w_in, b_in, rpb, w_att_o, conv_w, conv_b, w_rg_a, b_rg_a, w_rg_i, b_rg_i, lru_lambda, w_rec_o, w_out, ln2_g, w_ff1, w_ff2, lnf_g, loss_target, m_ln1_g, m_w_in, m_b_in, m_rpb, m_w_att_o, m_conv_w, m_conv_b, m_w_rg_a, m_b_rg_a, m_w_rg_i, m_b_rg_i, m_lru_lambda, m_w_rec_o, m_w_out, m_ln2_g, m_w_ff1, m_w_ff2, m_lnf_g, v_ln1_g, v_w_in, v_b_in, v_rpb, v_w_att_o, v_conv_w, v_conv_b, v_w_rg_a, v_b_rg_a, v_w_rg_i, v_b_rg_i, v_lru_lambda, v_w_rec_o, v_w_out, v_ln2_g, v_w_ff1, v_w_ff2, v_lnf_g):
    w = dict(zip(NAMES, (ln1_g, w_in, b_in, rpb, w_att_o, conv_w, conv_b, w_rg_a, b_rg_a, w_rg_i,
                         b_rg_i, lru_lambda, w_rec_o, w_out, ln2_g, w_ff1, w_ff2, lnf_g)))
    m = dict(zip(NAMES, (m_ln1_g, m_w_in, m_b_in, m_rpb, m_w_att_o, m_conv_w, m_conv_b, m_w_rg_a,
                         m_b_rg_a, m_w_rg_i, m_b_rg_i, m_lru_lambda, m_w_rec_o, m_w_out, m_ln2_g,
                         m_w_ff1, m_w_ff2, m_lnf_g)))
    v = dict(zip(NAMES, (v_ln1_g, v_w_in, v_b_in, v_rpb, v_w_att_o, v_conv_w, v_conv_b, v_w_rg_a,
                         v_b_rg_a, v_w_rg_i, v_b_rg_i, v_lru_lambda, v_w_rec_o, v_w_out, v_ln2_g,
                         v_w_ff1, v_w_ff2, v_lnf_g)))
    xi, yi, ci = _position()

    shard = {t: w[n][0].T.astype(BF16) for n, t in TRANSPOSED.items()}
    shard.update({n: w[n][0].astype(BF16) for n in ROW_SHARDED})
    shard["chan"] = _chan_bits([w[n][0] for n, _ in CHAN])
    first, later = ("w_in_t", "chan"), ("w_rec_o", "w_out", "w_ff1_t", "w_ff2", "w_att_o_t")
    p = dict(zip(first, _all_gather([shard[n] for n in first], "weight_all_gather")))
    send_sems, recv_sems, sent, zones, token = _gather_start([shard[n] for n in later], "weight_gather_start")

    def late_weights(after):
        landed = _gather_wait(send_sems, recv_sems, sent, zones, after, "weight_gather_wait")
        return dict(zip(later, _gather_pass_on([shard[n].shape[0] for n in later], landed,
                                               "weight_gather_pass_on")))

    chan = _chan_from_bits(p.pop("chan"))
    r0 = 0
    for n, rows in CHAN:
        p[n] = chan[r0:r0 + rows]
        r0 += rows
    p.update(ln1_g=w["ln1_g"], b_in=w["b_in"] + token[0, 0], rpb=w["rpb"][0], conv_b=w["conv_b"],
             w_rg_a=w["w_rg_a"][0], w_rg_i=w["w_rg_i"][0], ln2_g=w["ln2_g"],
             lnf_g=w["lnf_g"].reshape(1, D))

    core = jnp.reshape(ci, (1,)).astype(jnp.int32)
    chip = jnp.reshape(2 * xi + yi, (1,)).astype(jnp.int32)
    early_sections, late_sections = SECTIONS[1:], SECTIONS[:1]
    in_flight = {}

    def reduce_early(grads):
        chan_g = jnp.concatenate([grads[n] for n, _ in CHAN], axis=0)
        chan_g = chan_g.reshape(N_CHAN_ROWS, N_DEV, LANES).transpose(1, 0, 2).astype(BF16)
        chan_g = jnp.pad(chan_g.reshape(N_DEV, -1), ((0, 0), (0, CHAN_BLOCK_ROWS * D - N_CHAN_ROWS * LANES)))
        grads["chan"] = chan_g.reshape(N_DEV * CHAN_BLOCK_ROWS, D)
        sect = [grads[n] for n, _, _ in early_sections]
        got = _pair_exchange(early_sections, sect, "grad_pair_exchange_early")
        parts = _pair_add(early_sections, sect, got, core, "grad_pair_add_early")
        in_flight["early"] = _chip_exchange_start(early_sections, parts, "grad_chip_exchange_start")
        return in_flight["early"][-1][0, 0]

    loss_part, grad_x, grads = _local_step(x[0], loss_target[0], p, late_weights, reduce_early)
    sect = [grads[n] for n, _, _ in late_sections]
    got = _pair_exchange(late_sections, sect, "grad_pair_exchange_late")
    late_parts = _pair_add(late_sections, sect, got, core, "grad_pair_add_late")
    late_far = _chip_exchange(late_sections, late_parts, "grad_chip_exchange_late")
    send_sems, recv_sems, early_parts, zones, _ = in_flight["early"]
    early_parts, early_far = _chip_exchange_wait(early_sections, send_sems, recv_sems, early_parts, zones,
                                                 late_far[0], "grad_chip_exchange_wait")
    summed = dict(zip((n for n, _, _ in SECTIONS),
                      _grad_finish([*late_parts, *early_parts], [*late_far, *early_far], chip)))


    flat = jnp.concatenate([grads[n].reshape(-1) for n, _ in REPLICATED] + [loss_part.reshape(-1)])
    n_small = flat.shape[0]
    flat = jnp.pad(flat, (0, SMALL_ROWS * LANES - n_small)).reshape(SMALL_ROWS, LANES)
    (small_parts,) = _all_gather([flat], "small_grad_all_gather")
    small = _sum_devices(small_parts, SMALL_ROWS).reshape(-1)
    loss = small[n_small - 1]

    g, delta, new_m, new_v = {}, {}, {}, {}

    def update(n, g2, shape2):
        d2, m2, v2 = _adamw(w[n].reshape(shape2), g2, m[n].reshape(shape2), v[n].reshape(shape2),
                            "adamw_" + n)
        g[n], delta[n], new_m[n], new_v[n] = (a.reshape(w[n].shape) for a in (g2, d2, m2, v2))

    for n in ROW_SHARDED:
        update(n, summed[n], summed[n].shape)
    for n, t in TRANSPOSED.items():
        update(n, summed[t].T, summed[t].shape[::-1])
    chan_back = summed["chan"].reshape(-1)[:N_CHAN_ROWS * LANES].reshape(N_CHAN_ROWS, LANES)
    r0 = 0
    for n, rows in CHAN:
        update(n, chan_back[r0:r0 + rows], (rows, LANES))
        r0 += rows
    o = 0
    for n, shape2 in REPLICATED:
        size = shape2[0] * shape2[1]
        update(n, small[o:o + size].reshape(shape2), shape2)
        o += size

    return (loss, grad_x[None], *[g[n] for n in NAMES], *[delta[n] for n in NAMES],
            *[new_m[n] for n in NAMES], *[new_v[n] for n in NAMES])
```

```python
import math

import numpy as np
import jax
import jax.numpy as jnp
from jax import lax
from jax.experimental import pallas as pl
from jax.experimental.pallas import tpu as pltpu

F32 = jnp.float32
BF16 = jnp.bfloat16

T = 2048
D = 1024
D_ATT = 512
D_REC = 1024
D_FF = 4096
D_IN = 5632
N_HEADS = 8
DH = 64
GRID_W = 64
ROWS = T // GRID_W
WIN_H = 8
WIN_W = 16
KWIN = WIN_H * GRID_W
N_RPB_R = 2 * WIN_H - 1
N_RPB_C = 2 * WIN_W - 1
N_REC_BLOCKS = 16
REC_BLOCK = 64
CG = 128
N_CG = D_REC // CG
LRU_C = 8.0
EPS = 1e-6
N_DEV = 8
N_CHIPS = 4
LANES = 128

ADAM_LR = 0.001
ADAM_B1 = 0.9
ADAM_B2 = 0.999
ADAM_EPS = 1e-08
ADAM_WD = 0.01
ADAM_STEP = 10

MESH_AXES = ("x", "y", "c")
VMEM_LIMIT = 56 * 1024 * 1024

TILE = 512
DZ_ARRAYS = ((0, 3, 1), (3, 4, 2), (7, 4, 2))
N_DZ_TILES = D_IN // TILE


def _params(**kw):
    return pltpu.CompilerParams(vmem_limit_bytes=VMEM_LIMIT, **kw)


def _att_tables():
    rq = np.arange(2 * GRID_W) % GRID_W
    kc = np.arange(KWIN) % GRID_W
    win_start = np.clip(rq - WIN_W // 2, 0, GRID_W - WIN_W)
    valid = (kc[None, :] >= win_start[:, None]) & (kc[None, :] < win_start[:, None] + WIN_W)
    return valid.astype(np.float32), _pair_mask()


def _pair_mask():
    half = np.arange(2 * DH) // DH
    return (half[:, None] == half[None, :]).astype(np.float32)


def _dup_table():
    return np.concatenate([np.eye(REC_BLOCK, dtype=np.float32)] * 2, axis=1)


def _sigmoid(x):
    return 0.5 * jnp.tanh(0.5 * x) + 0.5


def _softplus(x):
    return jnp.maximum(x, 0.0) + jnp.log(1.0 + jnp.exp(-jnp.abs(x)))


def _one_minus_square(log_a, a):
    x = 2.0 * log_a
    series = -x * (1.0 + x * (0.5 + x * (1.0 / 6.0)))
    return jnp.where(x > -0.02, series, 1.0 - a * a)


_GELU_C = math.sqrt(2.0 / math.pi)


def _gelu_and_grad(x):
    x2 = x * x
    inner = _GELU_C * (x + 0.044715 * x * x2)
    t = jnp.tanh(inner)
    g = 0.5 * x * (1.0 + t)
    dg = 0.5 * (1.0 + t) + 0.5 * x * (1.0 - t * t) * _GELU_C * (1.0 + 3.0 * 0.044715 * x2)
    return g, dg


def _dot(a, b):
    return jnp.dot(a, b, preferred_element_type=F32)


def _dot_nt(a, b):
    return lax.dot_general(a, b, (((1,), (1,)), ((), ())), preferred_element_type=F32)


def _dot_tn(a, b):
    return lax.dot_general(a, b, (((0,), (0,)), ((), ())), preferred_element_type=F32)


def _dot_exact(a, b):
    return jnp.dot(a, b, precision=lax.Precision.HIGHEST, preferred_element_type=F32)


def _shift_rows(x, s):
    n = x.shape[0]
    rows = lax.broadcasted_iota(jnp.int32, x.shape, 0)
    y = pltpu.roll(x, s % n, 0)
    if s > 0:
        return jnp.where(rows >= s, y, 0.0)
    return jnp.where(rows < n + s, y, 0.0)


def _rms_bwd(dh, xh, r, g):
    dxh = dh * g
    return r * (dxh - xh * jnp.mean(dxh * xh, axis=-1, keepdims=True))


def _matmul(a, b, mode, out_dtype, name, tm=512, tn=1024, tk=2048):
    if mode == "nn":
        (m, k), (k2, n) = a.shape, b.shape
    elif mode == "nt":
        (m, k), (n, k2) = a.shape, b.shape
    else:
        (k, m), (k2, n) = a.shape, b.shape
    assert k == k2
    tm, tn, tk = min(tm, m), min(tn, n), min(tk, k)
    assert m % tm == 0 and n % tn == 0 and k % tk == 0
    nk = k // tk
    dot = {"nn": _dot, "nt": _dot_nt, "tn": _dot_tn}[mode]

    def body(a_ref, b_ref, o_ref, acc):
        kk = pl.program_id(2)
        part = dot(a_ref[...].astype(BF16), b_ref[...].astype(BF16))
        if nk == 1:
            o_ref[...] = part.astype(out_dtype)
            return

        @pl.when(kk == 0)
        def _():
            acc[...] = part

        @pl.when(kk > 0)
        def _():
            acc[...] += part

        @pl.when(kk == nk - 1)
        def _():
            o_ref[...] = acc[...].astype(out_dtype)

    if mode == "tn":
        a_spec = pl.BlockSpec((tk, tm), lambda i, j, kk: (kk, i))
    else:
        a_spec = pl.BlockSpec((tm, tk), lambda i, j, kk: (i, kk))
    if mode == "nt":
        b_spec = pl.BlockSpec((tn, tk), lambda i, j, kk: (j, kk))
    else:
        b_spec = pl.BlockSpec((tk, tn), lambda i, j, kk: (kk, j))
    return pl.pallas_call(
        body, name=name,
        out_shape=jax.ShapeDtypeStruct((m, n), out_dtype),
        grid=(m // tm, n // tn, nk),
        in_specs=[a_spec, b_spec],
        out_specs=pl.BlockSpec((tm, tn), lambda i, j, kk: (i, j)),
        scratch_shapes=[pltpu.VMEM((tm, tn) if nk > 1 else (8, LANES), F32)],
        compiler_params=_params(dimension_semantics=("parallel", "parallel", "arbitrary")),
    )(a, b)


def _in_proj(x, g1, w_in_t, b_in):
    tm = 1024

    def body(x_ref, g_ref, w_ref, b_ref, qkv_ref, uy_ref, gg_ref, h_ref, h_scr):
        j = pl.program_id(1)

        @pl.when(j == 0)
        def _():
            xv = x_ref[...]
            r = lax.rsqrt(jnp.mean(xv * xv, axis=-1, keepdims=True) + EPS)
            h = ((xv * r) * g_ref[...]).astype(BF16)
            h_scr[...] = h
            h_ref[...] = h

        z = _dot_nt(h_scr[...], w_ref[...]) + b_ref[...]

        @pl.when(j < 3)
        def _():
            qkv_ref[...] = z.astype(BF16)

        @pl.when((j >= 3) & (j < 7))
        def _():
            uy_ref[...] = z

        @pl.when(j >= 7)
        def _():
            gg_ref[...] = z

    return pl.pallas_call(
        body, name="in_proj",
        out_shape=(jax.ShapeDtypeStruct((T, 3 * D_ATT), BF16),
                   jax.ShapeDtypeStruct((T, 2 * D_REC), F32),
                   jax.ShapeDtypeStruct((T, 2 * D), F32),
                   jax.ShapeDtypeStruct((T, D), BF16)),
        grid=(T // tm, N_DZ_TILES),
        in_specs=[pl.BlockSpec((tm, D), lambda i, j: (i, 0)),
                  pl.BlockSpec((1, D), lambda i, j: (0, 0)),
                  pl.BlockSpec((TILE, D), lambda i, j: (j, 0)),
                  pl.BlockSpec((1, TILE), lambda i, j: (0, j))],
        out_specs=(pl.BlockSpec((tm, TILE), lambda i, j: (i, jnp.minimum(j, 2))),
                   pl.BlockSpec((tm, TILE), lambda i, j: (i, jnp.clip(j - 3, 0, 3))),
                   pl.BlockSpec((tm, TILE), lambda i, j: (i, jnp.clip(j - 7, 0, 3))),
                   pl.BlockSpec((tm, D), lambda i, j: (i, 0))),
        scratch_shapes=[pltpu.VMEM((tm, D), BF16)],
        compiler_params=_params(dimension_semantics=("parallel", "arbitrary")),
    )(x, g1, w_in_t, b_in)


def _dz_specs(rows, tile_of, row_of):
    def spec(off, n, per_plane):
        def index(*ids):
            t = jnp.clip(tile_of(*ids) - off, 0, n - 1)
            return (t // per_plane, row_of(*ids), t % per_plane)
        return pl.BlockSpec((1, rows, TILE), index)
    return [spec(off, n, per) for off, n, per in DZ_ARRAYS]


def _dh_norm1_bwd(dz, w_in_t, x, g1, dx1):
    tm = 1024

    def body(*refs):
        seg_refs = refs[:3]
        w_ref, x_ref, g_ref, dx1_ref, gx_ref, dg_ref, acc = refs[3:]
        i, kk = pl.program_id(0), pl.program_id(1)

        @pl.when(kk == 0)
        def _():
            acc[...] = jnp.zeros_like(acc)

        for s, (off, n, _) in enumerate(DZ_ARRAYS):
            @pl.when((kk >= off) & (kk < off + n))
            def _(s=s):
                acc[...] += _dot(seg_refs[s][0], w_ref[...])

        @pl.when((i == 0) & (kk == 0))
        def _():
            dg_ref[...] = jnp.zeros_like(dg_ref)

        @pl.when(kk == N_DZ_TILES - 1)
        def _():
            xv = x_ref[...]
            r = lax.rsqrt(jnp.mean(xv * xv, axis=-1, keepdims=True) + EPS)
            xh = xv * r
            dh = acc[...]
            dg_ref[...] += jnp.sum(dh * xh, axis=0, keepdims=True)
            gx_ref[...] = dx1_ref[...] + _rms_bwd(dh, xh, r, g_ref[...])

    tok = pl.BlockSpec((tm, D), lambda i, j: (i, 0))
    vec = pl.BlockSpec((1, D), lambda i, j: (0, 0))
    return pl.pallas_call(
        body, name="dh_norm1_bwd",
        out_shape=(jax.ShapeDtypeStruct((T, D), F32), jax.ShapeDtypeStruct((1, D), F32)),
        grid=(T // tm, N_DZ_TILES),
        in_specs=_dz_specs(tm, lambda i, j: j, lambda i, j: i)
        + [pl.BlockSpec((TILE, D), lambda i, j: (j, 0)), tok, vec, tok],
        out_specs=(tok, vec),
        scratch_shapes=[pltpu.VMEM((tm, D), F32)],
        compiler_params=_params(dimension_semantics=("arbitrary", "arbitrary")),
    )(*dz, w_in_t, x, g1, dx1)


def _grad_w_in(dz, h):
    def body(*refs):
        seg_refs = refs[:3]
        h_ref, gw_ref, gb_ref = refs[3:]
        j = pl.program_id(0)

        for s, (off, n, _) in enumerate(DZ_ARRAYS):
            @pl.when((j >= off) & (j < off + n))
            def _(s=s):
                a = seg_refs[s][0]
                gw_ref[...] = _dot_tn(a, h_ref[...]).astype(BF16)
                gb_ref[...] = jnp.sum(a.astype(F32), axis=0, keepdims=True)

    return pl.pallas_call(
        body, name="grad_w_in",
        out_shape=(jax.ShapeDtypeStruct((D_IN, D), BF16), jax.ShapeDtypeStruct((1, D_IN), F32)),
        grid=(N_DZ_TILES,),
        in_specs=_dz_specs(T, lambda j: j, lambda j: 0) + [pl.BlockSpec((T, D), lambda j: (0, 0))],
        out_specs=(pl.BlockSpec((TILE, D), lambda j: (j, 0)), pl.BlockSpec((1, TILE), lambda j: (0, j))),
        compiler_params=_params(dimension_semantics=("parallel",)),
    )(*dz, h)


def _rpb_rows(rpb):
    padded = jnp.pad(rpb, ((0, 0), (0, 0), (0, GRID_W - N_RPB_C)))
    rows = [padded[:, WIN_H - 1 - oi: 2 * WIN_H - 1 - oi].reshape(N_HEADS // 2, 2, KWIN)
            for oi in range(WIN_H)]
    return jnp.stack(rows, axis=0)


SKEW = KWIN - (WIN_W - 1)


def _bias_tiles(rows_ref, bias_s):
    for oi in range(WIN_H):
        for hh in range(2):
            row = jnp.broadcast_to(rows_ref[oi, 0, hh:hh + 1, :], (GRID_W, KWIN))
            bias_s[oi, hh * GRID_W:(hh + 1) * GRID_W, :] = pltpu.roll(row, SKEW, 1, stride=1, stride_axis=0)


def _bias_tile_grads(gb_s, flip, out_ref):
    for oi in range(WIN_H):
        for hh in range(2):
            g = _dot_exact(flip, gb_s[oi, hh * GRID_W:(hh + 1) * GRID_W, :])
            back = pltpu.roll(g, KWIN - (GRID_W - WIN_W), 1, stride=1, stride_axis=0)
            out_ref[0, oi, hh:hh + 1, :] = jnp.sum(back, axis=0, keepdims=True)


def _rpb_fold(row_grads):
    g = row_grads.transpose(1, 0, 2, 3).reshape(WIN_H, N_HEADS, WIN_H, GRID_W)
    g = g.transpose(0, 2, 1, 3)

    def body(g_ref, o_ref):
        for dr in range(N_RPB_R):
            terms = [g_ref[oi, i] for oi in range(WIN_H) for i in range(WIN_H) if i - oi + WIN_H - 1 == dr]
            acc = terms[0]
            for term in terms[1:]:
                acc = acc + term
            o_ref[dr] = acc

    out = pl.pallas_call(
        body, name="rpb_fold",
        out_shape=jax.ShapeDtypeStruct((N_RPB_R, N_HEADS, GRID_W), F32),
    )(g)
    return out.transpose(1, 0, 2)[:, :, :N_RPB_C]


def _att_scores(q_ref, k_ref, bias_ref, valid, hmask, r):
    rs = jnp.clip(r - WIN_H // 2, 0, ROWS - WIN_H)
    oi = r - rs
    q0 = pl.multiple_of(r * GRID_W, GRID_W)
    k0 = pl.multiple_of(rs * GRID_W, GRID_W)
    q_r = q_ref[pl.ds(q0, GRID_W), :]
    q2 = jnp.where(hmask, jnp.concatenate([q_r, q_r], axis=0), jnp.zeros((), BF16))
    kw = k_ref[pl.ds(k0, KWIN), :]
    s = _dot_nt(q2, kw) * (DH ** -0.5) + bias_ref[oi]
    s = jnp.where(valid, s, -1e30)
    m = jnp.max(s, axis=-1, keepdims=True)
    p = jnp.exp(s - m)
    p = p / jnp.sum(p, axis=-1, keepdims=True)
    return p, q2, kw, q0, k0, oi


def _att_fwd(qkv, bias_rows):
    valid_np, hmask_np = _att_tables()

    def body(q_ref, k_ref, v_ref, rows_ref, valid_ref, hmask_ref, o_ref, bias_s):
        valid = valid_ref[...] > 0.5
        hmask = hmask_ref[...] > 0.5
        first_head = lax.broadcasted_iota(jnp.int32, (GRID_W, 2 * DH), 1) < DH
        _bias_tiles(rows_ref, bias_s)

        def row(r, carry):
            p, _, _, q0, k0, _ = _att_scores(q_ref, k_ref, bias_s, valid, hmask, r)
            o2 = _dot(p.astype(BF16), v_ref[pl.ds(k0, KWIN), :])
            o_ref[pl.ds(q0, GRID_W), :] = jnp.where(first_head, o2[:GRID_W], o2[GRID_W:]).astype(BF16)
            return carry

        lax.fori_loop(0, ROWS, row, 0, unroll=4)

    col = lambda off: pl.BlockSpec((T, 2 * DH), lambda hp: (0, hp + off))
    return pl.pallas_call(
        body, name="att_fwd",
        out_shape=jax.ShapeDtypeStruct((T, D_ATT), BF16),
        grid=(N_HEADS // 2,),
        in_specs=[col(0), col(4), col(8),
                  pl.BlockSpec((WIN_H, 1, 2, KWIN), lambda hp: (0, hp, 0, 0)),
                  pl.BlockSpec((2 * GRID_W, KWIN), lambda hp: (0, 0)),
                  pl.BlockSpec((2 * DH, 2 * DH), lambda hp: (0, 0))],
        out_specs=pl.BlockSpec((T, 2 * DH), lambda hp: (0, hp)),
        scratch_shapes=[pltpu.VMEM((WIN_H, 2 * GRID_W, KWIN), F32)],
        compiler_params=_params(dimension_semantics=("parallel",)),
    )(qkv, qkv, qkv, bias_rows, jnp.asarray(valid_np), jnp.asarray(hmask_np))


def _att_bwd(qkv, bias_rows, datt, after):
    valid_np, hmask_np = _att_tables()

    def body(q_ref, k_ref, v_ref, do_ref, rows_ref, valid_ref, hmask_ref, flip_ref,
             dqkv_ref, grows_ref, dk_acc, dv_acc, bias_s, gb_s):
        valid = valid_ref[...] > 0.5
        hmask = hmask_ref[...] > 0.5
        first_head = lax.broadcasted_iota(jnp.int32, (GRID_W, 2 * DH), 1) < DH
        dk_acc[...] = jnp.zeros_like(dk_acc)
        dv_acc[...] = jnp.zeros_like(dv_acc)
        gb_s[...] = jnp.zeros_like(gb_s)
        _bias_tiles(rows_ref, bias_s)

        def row(r, carry):
            p, q2, kw, q0, k0, oi = _att_scores(q_ref, k_ref, bias_s, valid, hmask, r)
            do_r = do_ref[pl.ds(q0, GRID_W), :]
            do2 = jnp.where(hmask, jnp.concatenate([do_r, do_r], axis=0), jnp.zeros((), BF16))
            vw = v_ref[pl.ds(k0, KWIN), :]
            dp = _dot_nt(do2, vw)
            ds = p * (dp - jnp.sum(dp * p, axis=-1, keepdims=True))
            p16 = p.astype(BF16)
            ds16 = ds.astype(BF16)
            dv_acc[pl.ds(k0, KWIN), :] += _dot_tn(p16, do2)
            dk_acc[pl.ds(k0, KWIN), :] += _dot_tn(ds16, q2) * (DH ** -0.5)
            dq2 = _dot(ds16, kw) * (DH ** -0.5)
            dqkv_ref[0, pl.ds(q0, GRID_W), :] = jnp.where(first_head, dq2[:GRID_W], dq2[GRID_W:]).astype(BF16)
            gb_s[oi] += ds
            return carry

        lax.fori_loop(0, ROWS, row, 0, unroll=4)
        dqkv_ref[1] = dk_acc[...].astype(BF16)
        dqkv_ref[2] = dv_acc[...].astype(BF16)
        _bias_tile_grads(gb_s, flip_ref[...], grows_ref)

    col = lambda off: pl.BlockSpec((T, 2 * DH), lambda hp: (0, hp + off))
    tiles = pltpu.VMEM((WIN_H, 2 * GRID_W, KWIN), F32)
    return pl.pallas_call(
        body, name="att_bwd",
        out_shape=(jax.ShapeDtypeStruct((3, T, D_ATT), BF16),
                   jax.ShapeDtypeStruct((N_HEADS // 2, WIN_H, 2, KWIN), F32)),
        grid=(N_HEADS // 2,),
        in_specs=[col(0), col(4), col(8), col(0),
                  pl.BlockSpec((WIN_H, 1, 2, KWIN), lambda hp: (0, hp, 0, 0)),
                  pl.BlockSpec((2 * GRID_W, KWIN), lambda hp: (0, 0)),
                  pl.BlockSpec((2 * DH, 2 * DH), lambda hp: (0, 0)),
                  pl.BlockSpec((GRID_W, GRID_W), lambda hp: (0, 0))],
        out_specs=(pl.BlockSpec((3, T, 2 * DH), lambda hp: (0, 0, hp)),
                   pl.BlockSpec((1, WIN_H, 2, KWIN), lambda hp: (hp, 0, 0, 0))),
        scratch_shapes=[pltpu.VMEM((T, 2 * DH), F32), pltpu.VMEM((T, 2 * DH), F32), tiles, tiles],
        compiler_params=_params(dimension_semantics=("parallel",)),
    )(qkv, qkv, qkv, datt, bias_rows, jnp.asarray(valid_np) + after, jnp.asarray(hmask_np),
      jnp.asarray(np.eye(GRID_W, dtype=np.float32)[::-1].copy()))


def _conv_taps(up):
    return (_shift_rows(up, 2), _shift_rows(up, 1), up, _shift_rows(up, -1))


def _pair_block_diag(w_pair, dup, same_half):
    return jnp.where(same_half, _dot(w_pair.astype(BF16), dup), 0.0).astype(BF16)


def _gates(u, u16, wa, ba, wi, bi, lam):
    r = _sigmoid(_dot(u16, wa) + ba)
    ig = _sigmoid(_dot(u16, wi) + bi)
    sp = _softplus(-lam)
    log_a = (-LRU_C) * r * sp
    a = jnp.exp(log_a)
    mult = jnp.sqrt(jnp.maximum(_one_minus_square(log_a, a), 0.0))
    return r, ig, sp, a, mult


SCAN_BLOCKS = 2


def _scans(jobs):
    c = jobs[0][0].shape[1]
    nblk = T // 8
    rows = lax.broadcasted_iota(jnp.int32, (8, c), 0)

    def block(a, b, reverse):
        for s in (1, 2, 4):
            if reverse:
                keep = rows < 8 - s
                a_s = jnp.where(keep, pltpu.roll(a, 8 - s, 0), 1.0)
                b_s = jnp.where(keep, pltpu.roll(b, 8 - s, 0), 0.0)
            else:
                keep = rows >= s
                a_s = jnp.where(keep, pltpu.roll(a, s, 0), 1.0)
                b_s = jnp.where(keep, pltpu.roll(b, s, 0), 0.0)
            b = a * b_s + b
            a = a * a_s
        return a, b

    def step(i, carry):
        out = []
        for (a_ref, b_ref, h_ref, reverse), h_prev in zip(jobs, carry):
            for u in range(SCAN_BLOCKS):
                blk = i * SCAN_BLOCKS + u
                if reverse:
                    blk = nblk - 1 - blk
                t0 = pl.multiple_of(blk * 8, 8)
                a, b = block(a_ref[pl.ds(t0, 8), :], b_ref[pl.ds(t0, 8), :], reverse)
                h = a * h_prev + b
                h_ref[pl.ds(t0, 8), :] = h
                h_prev = jnp.broadcast_to(h[0:1] if reverse else h[7:8], (8, c))
            out.append(h_prev)
        return tuple(out)

    lax.fori_loop(0, nblk // SCAN_BLOCKS, step, tuple(jnp.zeros((8, c), F32) for _ in jobs))


def _rec_specs():
    tok = lambda off: pl.BlockSpec((T, CG), lambda g: (0, g + off))
    per_ch = lambda rows: pl.BlockSpec((rows, CG), lambda g: (0, g))
    wspec = pl.BlockSpec((2, 1, CG, REC_BLOCK), lambda g: (0, g, 0, 0))
    const = lambda shape: pl.BlockSpec(shape, lambda g: (0, 0))
    return tok, per_ch, wspec, const


def _rec_fwd(uy, conv_w, conv_b, w_a, b_a, w_i, b_i, lam):
    tok, per_ch, wspec, const = _rec_specs()

    def body(up_ref, yb_ref, cw_ref, cb_ref, wa_ref, ba_ref, wi_ref, bi_ref, lam_ref, dup_ref, half_ref,
             hf_ref, hb_ref, yrec_ref, a_f, bx_f, a_b, bx_b):
        dup = dup_ref[...]
        same_half = half_ref[...] > 0.5
        taps = _conv_taps(up_ref[...])
        u = cb_ref[...]
        for j in range(4):
            u = u + taps[j] * cw_ref[j:j + 1, :]
        u16 = u.astype(BF16)
        for d, (a_s, bx_s) in enumerate(((a_f, bx_f), (a_b, bx_b))):
            wa = _pair_block_diag(wa_ref[d, 0], dup, same_half)
            wi = _pair_block_diag(wi_ref[d, 0], dup, same_half)
            _, ig, _, a, mult = _gates(u, u16, wa, ba_ref[d:d + 1, :], wi, bi_ref[d:d + 1, :],
                                       lam_ref[d:d + 1, :])
            a_s[...] = a
            bx_s[...] = mult * (ig * u)
        _scans([(a_f, bx_f, hf_ref, False), (a_b, bx_b, hb_ref, True)])
        gelu, _ = _gelu_and_grad(yb_ref[...])
        yrec_ref[...] = ((hf_ref[...] + hb_ref[...]) * gelu).astype(BF16)

    return pl.pallas_call(
        body, name="rec_fwd",
        out_shape=(jax.ShapeDtypeStruct((T, D_REC), F32), jax.ShapeDtypeStruct((T, D_REC), F32),
                   jax.ShapeDtypeStruct((T, D_REC), BF16)),
        grid=(N_CG,),
        in_specs=[tok(0), tok(N_CG), per_ch(4), per_ch(1), wspec, per_ch(2), wspec, per_ch(2), per_ch(2),
                  const((REC_BLOCK, CG)), const((CG, CG))],
        out_specs=(tok(0), tok(0), tok(0)),
        scratch_shapes=[pltpu.VMEM((T, CG), F32)] * 4,
        compiler_params=_params(dimension_semantics=("parallel",)),
    )(uy, uy, conv_w, conv_b, w_a, b_a, w_i, b_i, lam,
      jnp.asarray(_dup_table(), BF16), jnp.asarray(_pair_mask()))


def _rec_bwd(uy, hf, hb, dyrec, conv_w, conv_b, w_a, b_a, w_i, b_i, lam):
    tok, per_ch, wspec, const = _rec_specs()

    def body(up_ref, yb_ref, hf_ref, hb_ref, dy_ref, cw_ref, cb_ref, wa_ref, ba_ref, wi_ref, bi_ref,
             lam_ref, dup_ref, dupt_ref, half_ref,
             duy_ref, dcw_ref, dcb_ref, dwa_ref, dba_ref, dwi_ref, dbi_ref, dlam_ref,
             a_s0, a_s1, dh_s, g_s0, g_s1):
        dup = dup_ref[...]
        dup_t = dupt_ref[...]
        same_half = half_ref[...] > 0.5
        taps = _conv_taps(up_ref[...])
        u = cb_ref[...]
        for j in range(4):
            u = u + taps[j] * cw_ref[j:j + 1, :]
        u16 = u.astype(BF16)
        gelu, dgelu = _gelu_and_grad(yb_ref[...])
        dy = dy_ref[...]
        duy_ref[1] = (dy * (hf_ref[...] + hb_ref[...]) * dgelu).astype(BF16)
        dh_s[...] = dy * gelu
        gate_values = []
        for d, a_s in enumerate((a_s0, a_s1)):
            wa = _pair_block_diag(wa_ref[d, 0], dup, same_half)
            wi = _pair_block_diag(wi_ref[d, 0], dup, same_half)
            lam_d = lam_ref[d:d + 1, :]
            r, ig, sp, a, mult = _gates(u, u16, wa, ba_ref[d:d + 1, :], wi, bi_ref[d:d + 1, :], lam_d)
            a_s[...] = _shift_rows(a, 1 if d == 1 else -1)
            gate_values.append((wa, wi, lam_d, r, ig, sp, a, mult))
        _scans([(a_s0, dh_s, g_s0, True), (a_s1, dh_s, g_s1, False)])
        du = jnp.zeros((T, CG), F32)
        for d, g_s in enumerate((g_s0, g_s1)):
            reverse = d == 1
            wa, wi, lam_d, r, ig, sp, a, mult = gate_values[d]
            g = g_s[...]
            h_prev = _shift_rows(hb_ref[...], -1) if reverse else _shift_rows(hf_ref[...], 1)
            da = g * h_prev
            dmult = g * (ig * u)
            dig = g * mult * u
            du = du + g * mult * ig
            dmult_dlog = jnp.where(mult > 0.0, -(a * a) / mult, 0.0)
            dlog_a = da * a + dmult * dmult_dlog
            dr = dlog_a * ((-LRU_C) * sp)
            dsp = jnp.sum(dlog_a * ((-LRU_C) * r), axis=0, keepdims=True)
            dlam_ref[d:d + 1, :] = dsp * (-_sigmoid(-lam_d))
            dga = dr * r * (1.0 - r)
            dgi = dig * ig * (1.0 - ig)
            dga16 = dga.astype(BF16)
            dgi16 = dgi.astype(BF16)
            du = du + _dot_nt(dga16, wa) + _dot_nt(dgi16, wi)
            dwa_ref[d, 0] = _dot_exact(jnp.where(same_half, _dot_tn(u16, dga16), 0.0), dup_t)
            dwi_ref[d, 0] = _dot_exact(jnp.where(same_half, _dot_tn(u16, dgi16), 0.0), dup_t)
            dba_ref[d:d + 1, :] = jnp.sum(dga, axis=0, keepdims=True)
            dbi_ref[d:d + 1, :] = jnp.sum(dgi, axis=0, keepdims=True)
        dcb_ref[...] = jnp.sum(du, axis=0, keepdims=True)
        for j in range(4):
            dcw_ref[j:j + 1, :] = jnp.sum(du * taps[j], axis=0, keepdims=True)
        dup_in = (_shift_rows(du, -2) * cw_ref[0:1, :] + _shift_rows(du, -1) * cw_ref[1:2, :]
                  + du * cw_ref[2:3, :] + _shift_rows(du, 1) * cw_ref[3:4, :])
        duy_ref[0] = dup_in.astype(BF16)

    wshape = jax.ShapeDtypeStruct((2, N_CG, CG, REC_BLOCK), F32)
    vec = lambda rows: jax.ShapeDtypeStruct((rows, D_REC), F32)
    dup_np = _dup_table()
    return pl.pallas_call(
        body, name="rec_bwd",
        out_shape=(jax.ShapeDtypeStruct((2, T, D_REC), BF16),
                   vec(4), vec(1), wshape, vec(2), wshape, vec(2), vec(2)),
        grid=(N_CG,),
        in_specs=[tok(0), tok(N_CG), tok(0), tok(0), tok(0),
                  per_ch(4), per_ch(1), wspec, per_ch(2), wspec, per_ch(2), per_ch(2),
                  const((REC_BLOCK, CG)), const((CG, REC_BLOCK)), const((CG, CG))],
        out_specs=(pl.BlockSpec((2, T, CG), lambda g: (0, 0, g)),
                   per_ch(4), per_ch(1), wspec, per_ch(2), wspec, per_ch(2), per_ch(2)),
        scratch_shapes=[pltpu.VMEM((T, CG), F32)] * 5,
        compiler_params=_params(dimension_semantics=("parallel",)),
    )(uy, uy, hf, hb, dyrec, conv_w, conv_b, w_a, b_a, w_i, b_i, lam,
      jnp.asarray(dup_np, BF16), jnp.asarray(dup_np.T.copy()), jnp.asarray(_pair_mask()))


TM_MIX = 256


def _mix_specs():
    tok = lambda width, blk=0: pl.BlockSpec((TM_MIX, width), lambda i: (i, blk))
    full = lambda shape: pl.BlockSpec(shape, lambda i: (0, 0))
    return tok, full


def _mix_fwd(x, att, yrec, gg, w_att_o_t, w_rec_o, w_out):
    tok, full = _mix_specs()

    def body(x_ref, att_ref, yr_ref, ga_ref, gr_ref, wao_ref, wro_ref, wo_ref, x1_ref, mixed_ref):
        y_att = _dot_nt(att_ref[...], wao_ref[...])
        y_rec = _dot(yr_ref[...], wro_ref[...])
        mixed = (_sigmoid(ga_ref[...]) * y_att + _sigmoid(gr_ref[...]) * y_rec).astype(BF16)
        mixed_ref[...] = mixed
        x1_ref[...] = x_ref[...] + _dot(mixed, wo_ref[...])

    return pl.pallas_call(
        body, name="mix_fwd",
        out_shape=(jax.ShapeDtypeStruct((T, D), F32), jax.ShapeDtypeStruct((T, D), BF16)),
        grid=(T // TM_MIX,),
        in_specs=[tok(D), tok(D_ATT), tok(D_REC), tok(D, 0), tok(D, 1),
                  full((D, D_ATT)), full((D_REC, D)), full((D, D))],
        out_specs=(tok(D), tok(D)),
        compiler_params=_params(dimension_semantics=("parallel",)),
    )(x, att, yrec, gg, gg, w_att_o_t, w_rec_o, w_out)


def _mix_bwd(dx1, att, yrec, gg, w_att_o_t, w_rec_o, w_out):
    tok, full = _mix_specs()

    def body(dx_ref, att_ref, yr_ref, ga_ref, gr_ref, wao_ref, wro_ref, wo_ref,
             dgg_ref, dya_ref, dyr_ref, datt_ref, dyrp_ref):
        dmixed = _dot_nt(dx_ref[...].astype(BF16), wo_ref[...])
        y_att = _dot_nt(att_ref[...], wao_ref[...])
        y_rec = _dot(yr_ref[...], wro_ref[...])
        sa = _sigmoid(ga_ref[...])
        sr = _sigmoid(gr_ref[...])
        dgg_ref[0] = (dmixed * y_att * sa * (1.0 - sa)).astype(BF16)
        dgg_ref[1] = (dmixed * y_rec * sr * (1.0 - sr)).astype(BF16)
        dya = (dmixed * sa).astype(BF16)
        dyr = (dmixed * sr).astype(BF16)
        dya_ref[...] = dya
        dyr_ref[...] = dyr
        datt_ref[...] = _dot(dya, wao_ref[...]).astype(BF16)
        dyrp_ref[...] = _dot_nt(dyr, wro_ref[...])

    return pl.pallas_call(
        body, name="mix_bwd",
        out_shape=(jax.ShapeDtypeStruct((2, T, D), BF16),
                   jax.ShapeDtypeStruct((T, D), BF16), jax.ShapeDtypeStruct((T, D), BF16),
                   jax.ShapeDtypeStruct((T, D_ATT), BF16), jax.ShapeDtypeStruct((T, D_REC), F32)),
        grid=(T // TM_MIX,),
        in_specs=[tok(D), tok(D_ATT), tok(D_REC), tok(D, 0), tok(D, 1),
                  full((D, D_ATT)), full((D_REC, D)), full((D, D))],
        out_specs=(pl.BlockSpec((2, TM_MIX, D), lambda i: (0, i, 0)),
                   tok(D), tok(D), tok(D_ATT), tok(D_REC)),
        compiler_params=_params(dimension_semantics=("parallel",)),
    )(dx1, att, yrec, gg, gg, w_att_o_t, w_rec_o, w_out)


TM_FFN = 256
FF_CHUNK = 1024


def _ffn_loss(x1, target, g2, gf, w_ff1_t, w_ff2):
    n_chunks = D_FF // FF_CHUNK

    def body(x1_ref, tg_ref, g2_ref, gf_ref, w1_hbm, w2_hbm,
             loss_ref, dx1_ref, h2_ref, act_ref, dpre_ref, dx2_ref, dg2_ref, dgf_ref,
             w1, w2, relu_s):
        i = pl.program_id(0)

        @pl.when(i == 0)
        def _():
            pltpu.sync_copy(w1_hbm, w1)
            pltpu.sync_copy(w2_hbm, w2)
            loss_ref[...] = jnp.zeros_like(loss_ref)
            dg2_ref[...] = jnp.zeros_like(dg2_ref)
            dgf_ref[...] = jnp.zeros_like(dgf_ref)

        x1v = x1_ref[...]
        r2 = lax.rsqrt(jnp.mean(x1v * x1v, axis=-1, keepdims=True) + EPS)
        xh2 = x1v * r2
        h2 = (xh2 * g2_ref[...]).astype(BF16)
        h2_ref[...] = h2
        x2 = x1v
        for c in range(n_chunks):
            ff = slice(c * FF_CHUNK, (c + 1) * FF_CHUNK)
            rl = jnp.maximum(_dot_nt(h2, w1[ff, :]), 0.0)
            relu_s[:, ff] = rl
            act = (rl * rl).astype(BF16)
            act_ref[:, ff] = act
            x2 = x2 + _dot(act, w2[ff, :])
        r3 = lax.rsqrt(jnp.mean(x2 * x2, axis=-1, keepdims=True) + EPS)
        xh3 = x2 * r3
        err = xh3 * gf_ref[...] - tg_ref[...]
        loss_ref[...] += 0.5 * jnp.sum(jnp.mean(err * err, axis=-1, keepdims=True))
        dy = err * (1.0 / D)
        dgf_ref[...] += jnp.sum(dy * xh3, axis=0, keepdims=True)
        dx2 = _rms_bwd(dy, xh3, r3, gf_ref[...])
        dx2_16 = dx2.astype(BF16)
        dx2_ref[...] = dx2_16
        dh2 = jnp.zeros((TM_FFN, D), F32)
        for c in range(n_chunks):
            ff = slice(c * FF_CHUNK, (c + 1) * FF_CHUNK)
            dpre = (_dot_nt(dx2_16, w2[ff, :]) * (2.0 * relu_s[:, ff])).astype(BF16)
            dpre_ref[:, ff] = dpre
            dh2 = dh2 + _dot(dpre, w1[ff, :])
        dg2_ref[...] += jnp.sum(dh2 * xh2, axis=0, keepdims=True)
        dx1_ref[...] = dx2 + _rms_bwd(dh2, xh2, r2, g2_ref[...])

    tok = lambda width: pl.BlockSpec((TM_FFN, width), lambda i: (i, 0))
    vec = pl.BlockSpec((1, D), lambda i: (0, 0))
    hbm = pl.BlockSpec(memory_space=pl.ANY)
    return pl.pallas_call(
        body, name="ffn_loss",
        out_shape=(jax.ShapeDtypeStruct((8, 128), F32), jax.ShapeDtypeStruct((T, D), F32),
                   jax.ShapeDtypeStruct((T, D), BF16), jax.ShapeDtypeStruct((T, D_FF), BF16),
                   jax.ShapeDtypeStruct((T, D_FF), BF16), jax.ShapeDtypeStruct((T, D), BF16),
                   jax.ShapeDtypeStruct((1, D), F32), jax.ShapeDtypeStruct((1, D), F32)),
        grid=(T // TM_FFN,),
        in_specs=[tok(D), tok(D), vec, vec, hbm, hbm],
        out_specs=(pl.BlockSpec((8, 128), lambda i: (0, 0)), tok(D), tok(D), tok(D_FF), tok(D_FF), tok(D),
                   vec, vec),
        scratch_shapes=[pltpu.VMEM((D_FF, D), BF16), pltpu.VMEM((D_FF, D), BF16),
                        pltpu.VMEM((TM_FFN, D_FF), F32)],
        compiler_params=_params(dimension_semantics=("arbitrary",)),
    )(x1, target, g2, gf, w_ff1_t, w_ff2)


def _local_step(x, target, p, late_weights, reduce_early):
    bias = _rpb_rows(p["rpb"])
    pairs = lambda w: w.reshape(2, N_CG, CG, REC_BLOCK)
    w_a, w_i = pairs(p["w_rg_a"]), pairs(p["w_rg_i"])
    rec_params = (p["conv_w"], p["conv_b"], w_a, p["b_rg_a"], w_i, p["b_rg_i"], p["lru_lambda"])

    qkv, uy, gg, h = _in_proj(x, p["ln1_g"], p["w_in_t"], p["b_in"])
    att = _att_fwd(qkv, bias)
    hf, hb, yrec = _rec_fwd(uy, *rec_params)
    p = {**p, **late_weights(yrec)}
    x1, mixed = _mix_fwd(x, att, yrec, gg, p["w_att_o_t"], p["w_rec_o"], p["w_out"])
    loss8, dx1, h2, act, dpre, dx2, g_ln2, g_lnf = _ffn_loss(
        x1, target, p["ln2_g"], p["lnf_g"], p["w_ff1_t"], p["w_ff2"])

    dgg, dya, dyr, datt, dyrp = _mix_bwd(dx1, att, yrec, gg, p["w_att_o_t"], p["w_rec_o"], p["w_out"])
    duy, g_cw, g_cb, g_wa, g_ba, g_wi, g_bi, g_lam = _rec_bwd(uy, hf, hb, dyrp, *rec_params)
    blocks = lambda g: g.reshape(2, N_REC_BLOCKS, REC_BLOCK, REC_BLOCK)
    grads = {
        "w_att_o_t": _matmul(dya, att, "tn", BF16, "g_w_att_o"),
        "conv_w": g_cw, "conv_b": g_cb, "w_rg_a": blocks(g_wa), "b_rg_a": g_ba,
        "w_rg_i": blocks(g_wi), "b_rg_i": g_bi, "lru_lambda": g_lam,
        "w_rec_o": _matmul(yrec, dyr, "tn", BF16, "g_w_rec_o"),
        "w_out": _matmul(mixed, dx1, "tn", BF16, "g_w_out"),
        "ln2_g": g_ln2,
        "w_ff1_t": _matmul(dpre, h2, "tn", BF16, "g_w_ff1"),
        "w_ff2": _matmul(act, dx2, "tn", BF16, "g_w_ff2"),
        "lnf_g": g_lnf,
    }
    after = reduce_early(grads)
    dqkv, gbias = _att_bwd(qkv, bias, datt, after)
    dz = (dqkv, duy, dgg)
    grad_x, g_ln1 = _dh_norm1_bwd(dz, p["w_in_t"], x, p["ln1_g"], dx1)
    g_w_in_t, g_b_in = _grad_w_in(dz, h)
    grads.update(ln1_g=g_ln1, w_in_t=g_w_in_t, b_in=g_b_in, rpb=_rpb_fold(gbias))
    return loss8[0:1, 0:1], grad_x, grads


MESH_ID = pl.DeviceIdType.MESH
ANY = pl.BlockSpec(memory_space=pl.ANY)

CHAN_BLOCK_ROWS = 32
SECTIONS = (("w_in_t", 704, D), ("w_rec_o", 128, D), ("w_out", 128, D), ("w_ff1_t", 512, D),
            ("w_ff2", 512, D), ("chan", CHAN_BLOCK_ROWS, D), ("w_att_o_t", 128, D_ATT))
N_SEC = len(SECTIONS)
N_CHAN_ROWS = 10
CHAN = (("conv_w", 4), ("b_rg_a", 2), ("b_rg_i", 2), ("lru_lambda", 2))


def _position():
    return lax.axis_index("x"), lax.axis_index("y"), lax.axis_index("c")


def _other_chips(x, y):
    return [(1 - x, y), (x, 1 - y), (1 - x, 1 - y)]


def _block_of(ref, dev, rows):
    return ref.at[pl.ds(pl.multiple_of(dev * rows, 16), rows)]


def _all_gather(shards, name):
    ns = len(shards)

    def body(*refs):
        x_refs, out_refs, done_ref = refs[:ns], refs[ns:2 * ns], refs[2 * ns]
        send_sems, recv_sems, local_sems = refs[2 * ns + 1:]
        done_ref[0, 0] = 0.0
        x, y, c = _position()
        me, sibling = (x, y, c), (x, y, 1 - c)
        chips = _other_chips(x, y)

        def rows(s, px, py, pc):
            return _block_of(out_refs[s], 4 * px + 2 * py + pc, shards[s].shape[0])

        def copy(k, s, block, to, from_shard=False):
            return pltpu.make_async_remote_copy(
                src_ref=x_refs[s] if from_shard else rows(s, *block), dst_ref=rows(s, *block),
                send_sem=send_sems.at[k * ns + s], recv_sem=recv_sems.at[k * ns + s],
                device_id=to, device_id_type=MESH_ID)

        sections = range(ns)
        mine = [pltpu.make_async_copy(x_refs[s], rows(s, *me), local_sems.at[s]) for s in sections]
        first = [copy(0, s, me, sibling, True) for s in sections]
        first += [copy(1 + j, s, me, (*chip, c), True) for j, chip in enumerate(chips) for s in sections]
        for cp in mine + first:
            cp.start()
        passed = []
        for j, chip in enumerate(chips):
            for s in sections:
                copy(1 + j, s, (*chip, c), me).wait_recv()
                passed.append(copy(4 + j, s, (*chip, c), sibling))
                passed[-1].start()
        for s in sections:
            copy(0, s, sibling, me).wait_recv()
        for j, chip in enumerate(chips):
            for s in sections:
                copy(4 + j, s, (*chip, 1 - c), me).wait_recv()
        for cp in first + passed:
            cp.wait_send()
        for cp in mine:
            cp.wait()

    return pl.pallas_call(
        body, name=name,
        out_shape=tuple(jax.ShapeDtypeStruct((N_DEV * s.shape[0], s.shape[1]), s.dtype) for s in shards)
        + (jax.ShapeDtypeStruct((1, 1), F32),),
        in_specs=[ANY] * ns,
        out_specs=(ANY,) * ns + (pl.BlockSpec(memory_space=pltpu.SMEM),),
        scratch_shapes=[pltpu.SemaphoreType.DMA((7 * ns,)), pltpu.SemaphoreType.DMA((7 * ns,)),
                        pltpu.SemaphoreType.DMA((ns,))],
    )(*shards)


HBM = pl.BlockSpec(memory_space=pltpu.HBM)
SEM = pl.BlockSpec(memory_space=pltpu.SEMAPHORE)
EFFECT = pltpu.SideEffectType.DATAFLOW_SIDE_EFFECTING


def _in_hbm(a):
    return pltpu.with_memory_space_constraint(a, pltpu.HBM)


def _first_hop_copies(shards, x_refs, zones, send_sems, recv_sems):
    ns = len(shards)
    x, y, c = _position()
    targets = [(x, y, 1 - c)] + [(cx, cy, c) for cx, cy in _other_chips(x, y)]
    return [pltpu.make_async_remote_copy(
        src_ref=x_refs[s], dst_ref=_block_of(zones[s], 4 * x + 2 * y + c, shards[s].shape[0]),
        send_sem=send_sems.at[k * ns + s], recv_sem=recv_sems.at[k * ns + s],
        device_id=to, device_id_type=MESH_ID)
        for k, to in enumerate(targets) for s in range(ns)]


def _own_blocks_placed(shards, after):
    ns = len(shards)
    x, y, c = _position()
    me = jnp.reshape(4 * x + 2 * y + c, (1,)).astype(jnp.int32)
    shards = [*shards[:-1], shards[-1] + after.astype(shards[-1].dtype)]

    def body(me_ref, *refs):
        for s in range(ns):
            refs[ns + s][...] = refs[s][...]

    return pl.pallas_call(
        body, name="own_blocks_placed",
        out_shape=tuple(jax.ShapeDtypeStruct((N_DEV * s.shape[0], s.shape[1]), s.dtype) for s in shards),
        grid_spec=pltpu.PrefetchScalarGridSpec(
            num_scalar_prefetch=1, grid=(1,),
            in_specs=[pl.BlockSpec(s.shape, lambda i, me: (0, 0)) for s in shards],
            out_specs=tuple(pl.BlockSpec(s.shape, lambda i, me: (me[0], 0)) for s in shards)),
        compiler_params=_params(dimension_semantics=("arbitrary",)),
    )(me, *shards)


def _gather_start(shards, after, name):
    ns = len(shards)
    zones = _own_blocks_placed(shards, after)

    def body(*refs):
        for cp in _first_hop_copies(shards, refs[:ns], refs[ns:2 * ns], refs[2 * ns], refs[2 * ns + 1]):
            cp.start()
        refs[-1][...] = jnp.zeros_like(refs[-1])

    out = pl.pallas_call(
        body, name=name,
        out_shape=(pltpu.SemaphoreType.DMA((4 * ns,)), pltpu.SemaphoreType.DMA((4 * ns,)),
                   *[pltpu.HBM(a.shape, a.dtype) for a in (*shards, *zones)],
                   jax.ShapeDtypeStruct((8, LANES), F32)),
        in_specs=[HBM] * (2 * ns),
        out_specs=(SEM, SEM, *[HBM] * (2 * ns), pl.BlockSpec(memory_space=pltpu.VMEM)),
        input_output_aliases={i: 2 + i for i in range(2 * ns)},
        compiler_params=pltpu.CompilerParams(has_side_effects=EFFECT),
    )(*[_in_hbm(a) for a in shards], *[_in_hbm(a) for a in zones])
    return out[0], out[1], out[2:2 + ns], out[2 + ns:2 + 2 * ns], out[-1]


def _gather_wait(send_sems, recv_sems, shards, zones, after, name):
    ns = len(shards)

    def body(*refs):
        for cp in _first_hop_copies(shards, refs[:ns], refs[ns:2 * ns], refs[2 * ns], refs[2 * ns + 1]):
            cp.wait_send()
            cp.wait_recv()

    out = pl.pallas_call(
        body, name=name,
        out_shape=tuple(pltpu.HBM(a.shape, a.dtype) for a in (*shards, *zones)),
        in_specs=[HBM] * (2 * ns) + [SEM, SEM, ANY],
        out_specs=(HBM,) * (2 * ns),
        input_output_aliases={i: i for i in range(2 * ns)},
        compiler_params=pltpu.CompilerParams(has_side_effects=EFFECT),
    )(*shards, *zones, send_sems, recv_sems, after)
    return out[ns:]


def _gather_pass_on(rows, zones, name):
    ns = len(zones)

    def body(*refs):
        in_refs, out_refs = refs[:ns], refs[ns:2 * ns]
        send_sems, recv_sems = refs[2 * ns:]
        x, y, c = _position()
        copies = [pltpu.make_async_remote_copy(
            src_ref=_block_of(in_refs[s], 4 * cx + 2 * cy + c, rows[s]),
            dst_ref=_block_of(out_refs[s], 4 * cx + 2 * cy + c, rows[s]),
            send_sem=send_sems.at[j * ns + s], recv_sem=recv_sems.at[j * ns + s],
            device_id=(x, y, 1 - c), device_id_type=MESH_ID)
            for j, (cx, cy) in enumerate(_other_chips(x, y)) for s in range(ns)]
        for cp in copies:
            cp.start()
        for cp in copies:
            cp.wait_recv()
        for cp in copies:
            cp.wait_send()

    return pl.pallas_call(
        body, name=name,
        out_shape=tuple(jax.ShapeDtypeStruct(z.shape, z.dtype) for z in zones),
        in_specs=[ANY] * ns, out_specs=(ANY,) * ns,
        input_output_aliases={i: i for i in range(ns)},
        scratch_shapes=[pltpu.SemaphoreType.DMA((3 * ns,)), pltpu.SemaphoreType.DMA((3 * ns,))],
    )(*zones)


def _pair_exchange(sections, grads, name):
    ns = len(sections)

    def body(*refs):
        g_refs, land = refs[:ns], refs[ns:2 * ns]
        send_sems, recv_sems = refs[2 * ns:]
        x, y, c = _position()
        copies = [pltpu.make_async_remote_copy(
            src_ref=_block_of(g_refs[s], 2 * k + 1 - c, rows), dst_ref=land[s].at[k],
            send_sem=send_sems.at[k * ns + s], recv_sem=recv_sems.at[k * ns + s],
            device_id=(x, y, 1 - c), device_id_type=MESH_ID)
            for k in range(N_CHIPS) for s, (_, rows, _) in enumerate(sections)]
        for cp in copies:
            cp.start()
        for cp in copies:
            cp.wait_recv()
        for cp in copies:
            cp.wait_send()

    n = N_CHIPS * ns
    return pl.pallas_call(
        body, name=name,
        out_shape=tuple(jax.ShapeDtypeStruct((N_CHIPS, rows, cols), BF16) for _, rows, cols in sections),
        in_specs=[ANY] * ns, out_specs=(ANY,) * ns,
        scratch_shapes=[pltpu.SemaphoreType.DMA((n,)), pltpu.SemaphoreType.DMA((n,))],
    )(*grads)


def _pair_add(sections, grads, got, core, name):
    ns = len(sections)

    def body(core_ref, *refs):
        g_refs, got_refs, p_refs = refs[:ns], refs[ns:2 * ns], refs[2 * ns:]
        for s in range(ns):
            p_refs[s][0] = (g_refs[s][...].astype(F32) + got_refs[s][0].astype(F32)).astype(BF16)

    slot = [pl.BlockSpec((1, rows, cols), lambda k, c: (k, 0, 0)) for _, rows, cols in sections]
    return pl.pallas_call(
        body, name=name,
        out_shape=tuple(jax.ShapeDtypeStruct((N_CHIPS, rows, cols), BF16) for _, rows, cols in sections),
        grid_spec=pltpu.PrefetchScalarGridSpec(
            num_scalar_prefetch=1, grid=(N_CHIPS,),
            in_specs=[pl.BlockSpec((rows, cols), lambda k, c: (2 * k + c[0], 0)) for _, rows, cols in sections]
            + slot,
            out_specs=tuple(slot)),
        compiler_params=_params(dimension_semantics=("parallel",)),
    )(core, *grads, *got)


def _chip_copies(sections, p_refs, land, send_sems, recv_sems):
    ns = len(sections)
    x, y, c = _position()
    return [pltpu.make_async_remote_copy(
        src_ref=p_refs[s].at[2 * cx + cy], dst_ref=land[s].at[j],
        send_sem=send_sems.at[j * ns + s], recv_sem=recv_sems.at[j * ns + s],
        device_id=(cx, cy, c), device_id_type=MESH_ID)
        for j, (cx, cy) in enumerate(_other_chips(x, y)) for s in range(ns)]


def _chip_exchange(sections, parts, name):
    ns = len(sections)

    def body(*refs):
        copies = _chip_copies(sections, refs[:ns], refs[ns:2 * ns], *refs[2 * ns:])
        for cp in copies:
            cp.start()
        for cp in copies:
            cp.wait_recv()
        for cp in copies:
            cp.wait_send()

    n = 3 * ns
    return pl.pallas_call(
        body, name=name,
        out_shape=tuple(jax.ShapeDtypeStruct((3, rows, cols), BF16) for _, rows, cols in sections),
        in_specs=[ANY] * ns, out_specs=(ANY,) * ns,
        scratch_shapes=[pltpu.SemaphoreType.DMA((n,)), pltpu.SemaphoreType.DMA((n,))],
    )(*parts)


def _chip_exchange_start(sections, parts, name):
    ns = len(sections)

    def body(*refs):
        p_refs, land = refs[:ns], refs[ns:2 * ns]
        send_sems, recv_sems = refs[2 * ns], refs[2 * ns + 1]
        token = refs[-1]
        for cp in _chip_copies(sections, p_refs, land, send_sems, recv_sems):
            cp.start()
        token[...] = jnp.zeros_like(token)

    zones = [lax.empty((3, rows, cols), BF16) for _, rows, cols in sections]
    out = pl.pallas_call(
        body, name=name,
        out_shape=(pltpu.SemaphoreType.DMA((3 * ns,)), pltpu.SemaphoreType.DMA((3 * ns,)),
                   *[pltpu.HBM(a.shape, a.dtype) for a in parts], *[pltpu.HBM(a.shape, a.dtype) for a in zones],
                   jax.ShapeDtypeStruct((8, LANES), F32)),
        in_specs=[HBM] * (2 * ns),
        out_specs=(SEM, SEM, *[HBM] * (2 * ns), pl.BlockSpec(memory_space=pltpu.VMEM)),
        input_output_aliases={i: 2 + i for i in range(2 * ns)},
        compiler_params=pltpu.CompilerParams(has_side_effects=EFFECT),
    )(*[_in_hbm(a) for a in parts], *[_in_hbm(a) for a in zones])
    return out[0], out[1], out[2:2 + ns], out[2 + ns:2 + 2 * ns], out[-1]


def _chip_exchange_wait(sections, send_sems, recv_sems, parts, zones, after, name):
    ns = len(sections)

    def body(*refs):
        p_refs, land = refs[:ns], refs[ns:2 * ns]
        for cp in _chip_copies(sections, p_refs, land, refs[2 * ns], refs[2 * ns + 1]):
            cp.wait_send()
            cp.wait_recv()

    out = pl.pallas_call(
        body, name=name,
        out_shape=tuple(pltpu.HBM(a.shape, a.dtype) for a in (*parts, *zones)),
        in_specs=[HBM] * (2 * ns) + [SEM, SEM, ANY],
        out_specs=(HBM,) * (2 * ns),
        input_output_aliases={i: i for i in range(2 * ns)},
        compiler_params=pltpu.CompilerParams(has_side_effects=EFFECT),
    )(*parts, *zones, send_sems, recv_sems, after)
    return out[:ns], out[ns:]


def _grad_finish(parts, far, chip):
    def body(chip_ref, *refs):
        p_refs, b_refs, g_refs = refs[:N_SEC], refs[N_SEC:2 * N_SEC], refs[2 * N_SEC:]
        for s in range(N_SEC):
            g = p_refs[s][0].astype(F32)
            for j in range(3):
                g = g + b_refs[s][j].astype(F32)
            g_refs[s][...] = g

    half = [(rows // 2, cols) for _, rows, cols in SECTIONS]
    return pl.pallas_call(
        body, name="grad_finish",
        out_shape=tuple(jax.ShapeDtypeStruct((rows, cols), F32) for _, rows, cols in SECTIONS),
        grid_spec=pltpu.PrefetchScalarGridSpec(
            num_scalar_prefetch=1, grid=(2,),
            in_specs=[pl.BlockSpec((1, r, c), lambda i, chip: (chip[0], i, 0)) for r, c in half]
            + [pl.BlockSpec((3, r, c), lambda i, chip: (0, i, 0)) for r, c in half],
            out_specs=tuple(pl.BlockSpec((r, c), lambda i, chip: (i, 0)) for r, c in half)),
        compiler_params=_params(dimension_semantics=("parallel",)),
    )(chip, *parts, *far)


def _sum_devices(parts, rows, name):
    cols = parts.shape[1]
    tr = rows // 2

    def body(*refs):
        s = refs[0][...].astype(F32)
        for d in range(1, N_DEV):
            s = s + refs[d][...].astype(F32)
        refs[N_DEV][...] = s

    return pl.pallas_call(
        body, name=name,
        out_shape=jax.ShapeDtypeStruct((rows, cols), F32),
        grid=(2,),
        in_specs=[pl.BlockSpec((tr, cols), lambda i, d=d: (2 * d + i, 0)) for d in range(N_DEV)],
        out_specs=pl.BlockSpec((tr, cols), lambda i: (i, 0)),
        compiler_params=_params(dimension_semantics=("parallel",)),
    )(*([parts] * N_DEV))


def _adamw(w, g, m, v, name):
    rows, cols = w.shape
    tr = rows
    while tr * cols * 4 > (1 << 20) and tr % 16 == 0:
        tr //= 2
    c1 = 1.0 / (1.0 - ADAM_B1 ** ADAM_STEP)
    c2 = 1.0 / (1.0 - ADAM_B2 ** ADAM_STEP)

    def body(w_ref, g_ref, m_ref, v_ref, d_ref, nm_ref, nv_ref):
        gv = g_ref[...]
        nm = ADAM_B1 * m_ref[...] + (1.0 - ADAM_B1) * gv
        nv = ADAM_B2 * v_ref[...] + (1.0 - ADAM_B2) * (gv * gv)
        nm_ref[...] = nm
        nv_ref[...] = nv
        d_ref[...] = (-ADAM_LR) * ((nm * c1) / (jnp.sqrt(nv * c2) + ADAM_EPS) + ADAM_WD * w_ref[...])

    spec = pl.BlockSpec((tr, cols), lambda i: (i, 0))
    shape = jax.ShapeDtypeStruct((rows, cols), F32)
    return pl.pallas_call(
        body, name=name,
        out_shape=(shape, shape, shape),
        grid=(rows // tr,),
        in_specs=[spec] * 4, out_specs=(spec,) * 3,
        compiler_params=_params(dimension_semantics=("parallel",)),
    )(w, g, m, v)


NAMES = ("ln1_g", "w_in", "b_in", "rpb", "w_att_o", "conv_w", "conv_b", "w_rg_a", "b_rg_a", "w_rg_i",
         "b_rg_i", "lru_lambda", "w_rec_o", "w_out", "ln2_g", "w_ff1", "w_ff2", "lnf_g")
TRANSPOSED = {"w_in": "w_in_t", "w_att_o": "w_att_o_t", "w_ff1": "w_ff1_t"}
ROW_SHARDED = ("w_rec_o", "w_out", "w_ff2")
REPLICATED = (("ln1_g", (1, D)), ("b_in", (1, D_IN)), ("rpb", (N_HEADS * N_RPB_R, N_RPB_C)),
              ("conv_b", (1, D_REC)), ("w_rg_a", (2 * N_REC_BLOCKS * REC_BLOCK, REC_BLOCK)),
              ("w_rg_i", (2 * N_REC_BLOCKS * REC_BLOCK, REC_BLOCK)), ("ln2_g", (1, D)), ("lnf_g", (1, D)))
GATE_BLOCKS = ("w_rg_a", "w_rg_i")
SMALL_ROWS = 112


def _chan_bits(vectors):
    chan = jnp.concatenate(vectors, axis=0)
    bits = lax.bitcast_convert_type(chan, BF16).reshape(-1)
    return jnp.pad(bits, (0, CHAN_BLOCK_ROWS * D - bits.shape[0])).reshape(CHAN_BLOCK_ROWS, D)


def _chan_from_bits(gathered):
    bits = gathered.reshape(N_DEV, CHAN_BLOCK_ROWS * D)[:, :2 * N_CHAN_ROWS * LANES]
    chan = lax.bitcast_convert_type(bits.reshape(N_DEV, N_CHAN_ROWS, LANES, 2), F32)
    return chan.transpose(1, 0, 2).reshape(N_CHAN_ROWS, D)


def kernel(x, ln1_g, w_in, b_in, rpb, w_att_o, conv_w, conv_b, w_rg_a, b_rg_a, w_rg_i, b_rg_i, lru_lambda, w_rec_o, w_out, ln2_g, w_ff1, w_ff2, lnf_g, loss_target, m_ln1_g, m_w_in, m_b_in, m_rpb, m_w_att_o, m_conv_w, m_conv_b, m_w_rg_a, m_b_rg_a, m_w_rg_i, m_b_rg_i, m_lru_lambda, m_w_rec_o, m_w_out, m_ln2_g, m_w_ff1, m_w_ff2, m_lnf_g, v_ln1_g, v_w_in, v_b_in, v_rpb, v_w_att_o, v_conv_w, v_conv_b, v_w_rg_a, v_b_rg_a, v_w_rg_i, v_b_rg_i, v_lru_lambda, v_w_rec_o, v_w_out, v_ln2_g, v_w_ff1, v_w_ff2, v_lnf_g):
    w = dict(zip(NAMES, (ln1_g, w_in, b_in, rpb, w_att_o, conv_w, conv_b, w_rg_a, b_rg_a, w_rg_i,
                         b_rg_i, lru_lambda, w_rec_o, w_out, ln2_g, w_ff1, w_ff2, lnf_g)))
    m = dict(zip(NAMES, (m_ln1_g, m_w_in, m_b_in, m_rpb, m_w_att_o, m_conv_w, m_conv_b, m_w_rg_a,
                         m_b_rg_a, m_w_rg_i, m_b_rg_i, m_lru_lambda, m_w_rec_o, m_w_out, m_ln2_g,
                         m_w_ff1, m_w_ff2, m_lnf_g)))
    v = dict(zip(NAMES, (v_ln1_g, v_w_in, v_b_in, v_rpb, v_w_att_o, v_conv_w, v_conv_b, v_w_rg_a,
                         v_b_rg_a, v_w_rg_i, v_b_rg_i, v_lru_lambda, v_w_rec_o, v_w_out, v_ln2_g,
                         v_w_ff1, v_w_ff2, v_lnf_g)))
    xi, yi, ci = _position()

    shard = {t: w[n][0].T.astype(BF16) for n, t in TRANSPOSED.items()}
    shard.update({n: w[n][0].astype(BF16) for n in ROW_SHARDED})
    shard["chan"] = _chan_bits([w[n][0] for n, _ in CHAN])
    first, later = ("w_in_t", "chan"), ("w_rec_o", "w_out", "w_ff1_t", "w_ff2", "w_att_o_t")
    *gathered, done = _all_gather([shard[n] for n in first], "weight_all_gather")
    p = dict(zip(first, gathered))
    send_sems, recv_sems, sent, zones, token = _gather_start([shard[n] for n in later], done,
                                                             "weight_gather_start")

    def late_weights(after):
        landed = _gather_wait(send_sems, recv_sems, sent, zones, after, "weight_gather_wait")
        return dict(zip(later, _gather_pass_on([shard[n].shape[0] for n in later], landed,
                                               "weight_gather_pass_on")))

    chan = _chan_from_bits(p.pop("chan"))
    r0 = 0
    for n, rows in CHAN:
        p[n] = chan[r0:r0 + rows]
        r0 += rows
    p.update(ln1_g=w["ln1_g"], b_in=w["b_in"] + token[0, 0], rpb=w["rpb"][0], conv_b=w["conv_b"],
             w_rg_a=w["w_rg_a"][0], w_rg_i=w["w_rg_i"][0], ln2_g=w["ln2_g"],
             lnf_g=w["lnf_g"].reshape(1, D))

    core = jnp.reshape(ci, (1,)).astype(jnp.int32)
    chip = jnp.reshape(2 * xi + yi, (1,)).astype(jnp.int32)
    early_sections, late_sections = SECTIONS[1:], SECTIONS[:1]
    in_flight = {}

    def reduce_early(grads):
        chan_g = jnp.concatenate([grads[n] for n, _ in CHAN], axis=0)
        chan_g = chan_g.reshape(N_CHAN_ROWS, N_DEV, LANES).transpose(1, 0, 2).astype(BF16)
        chan_g = jnp.pad(chan_g.reshape(N_DEV, -1), ((0, 0), (0, CHAN_BLOCK_ROWS * D - N_CHAN_ROWS * LANES)))
        grads["chan"] = chan_g.reshape(N_DEV * CHAN_BLOCK_ROWS, D)
        sect = [grads[n] for n, _, _ in early_sections]
        got = _pair_exchange(early_sections, sect, "grad_pair_exchange_early")
        parts = _pair_add(early_sections, sect, got, core, "grad_pair_add_early")
        in_flight["early"] = _chip_exchange_start(early_sections, parts, "grad_chip_exchange_start")
        return in_flight["early"][-1][0, 0]

    loss_part, grad_x, grads = _local_step(x[0], loss_target[0], p, late_weights, reduce_early)
    sect = [grads[n] for n, _, _ in late_sections]
    got = _pair_exchange(late_sections, sect, "grad_pair_exchange_late")
    late_parts = _pair_add(late_sections, sect, got, core, "grad_pair_add_late")
    late_far = _chip_exchange(late_sections, late_parts, "grad_chip_exchange_late")
    send_sems, recv_sems, early_parts, zones, _ = in_flight["early"]
    early_parts, early_far = _chip_exchange_wait(early_sections, send_sems, recv_sems, early_parts, zones,
                                                 late_far[0], "grad_chip_exchange_wait")
    summed = dict(zip((n for n, _, _ in SECTIONS),
                      _grad_finish([*late_parts, *early_parts], [*late_far, *early_far], chip)))


    flat = jnp.concatenate([grads[n].reshape(-1) for n, _ in REPLICATED if n not in GATE_BLOCKS]
                           + [loss_part.reshape(-1)])
    n_small = flat.shape[0]
    flat = jnp.pad(flat, (0, SMALL_ROWS * LANES - n_small)).reshape(SMALL_ROWS, LANES)
    gates = jnp.concatenate([grads[n].reshape(-1, LANES) for n in GATE_BLOCKS], axis=0).astype(BF16)
    small_parts, gate_parts, _ = _all_gather([flat, gates], "small_grad_all_gather")
    small = _sum_devices(small_parts, SMALL_ROWS, "small_grad_sum").reshape(-1)
    gate_sum = _sum_devices(gate_parts, gates.shape[0], "gate_block_grad_sum")
    loss = small[n_small - 1]

    g, delta, new_m, new_v = {}, {}, {}, {}

    def update(n, g2, shape2):
        d2, m2, v2 = _adamw(w[n].reshape(shape2), g2, m[n].reshape(shape2), v[n].reshape(shape2),
                            "adamw_" + n)
        g[n], delta[n], new_m[n], new_v[n] = (a.reshape(w[n].shape) for a in (g2, d2, m2, v2))

    for n in ROW_SHARDED:
        update(n, summed[n], summed[n].shape)
    for n, t in TRANSPOSED.items():
        update(n, summed[t].T, summed[t].shape[::-1])
    chan_back = summed["chan"].reshape(-1)[:N_CHAN_ROWS * LANES].reshape(N_CHAN_ROWS, LANES)
    r0 = 0
    for n, rows in CHAN:
        update(n, chan_back[r0:r0 + rows], (rows, LANES))
        r0 += rows
    o = 0
    for n, shape2 in REPLICATED:
        if n in GATE_BLOCKS:
            k, rows = GATE_BLOCKS.index(n), gate_sum.shape[0] // len(GATE_BLOCKS)
            update(n, gate_sum[k * rows:(k + 1) * rows].reshape(shape2), shape2)
        else:
            size = shape2[0] * shape2[1]
            update(n, small[o:o + size].reshape(shape2), shape2)
            o += size

    return (loss, grad_x[None], *[g[n] for n in NAMES], *[delta[n] for n in NAMES],
            *[new_m[n] for n in NAMES], *[new_v[n] for n in NAMES])
```

```python
import math

import numpy as np
import jax
import jax.numpy as jnp
from jax import lax
from jax.experimental import pallas as pl
from jax.experimental.pallas import tpu as pltpu

F32 = jnp.float32
BF16 = jnp.bfloat16

T = 2048
D = 1024
D_ATT = 512
D_REC = 1024
D_FF = 4096
D_IN = 5632
N_HEADS = 8
DH = 64
GRID_W = 64
ROWS = T // GRID_W
WIN_H = 8
WIN_W = 16
KWIN = WIN_H * GRID_W
N_RPB_R = 2 * WIN_H - 1
N_RPB_C = 2 * WIN_W - 1
N_REC_BLOCKS = 16
REC_BLOCK = 64
CG = 128
N_CG = D_REC // CG
LRU_C = 8.0
EPS = 1e-6
N_DEV = 8
N_CHIPS = 4
LANES = 128

ADAM_LR = 0.001
ADAM_B1 = 0.9
ADAM_B2 = 0.999
ADAM_EPS = 1e-08
ADAM_WD = 0.01
ADAM_STEP = 10

MESH_AXES = ("x", "y", "c")
VMEM_LIMIT = 56 * 1024 * 1024

TILE = 512
DZ_ARRAYS = ((0, 3, 1), (3, 4, 2), (7, 4, 2))
N_DZ_TILES = D_IN // TILE


def _params(**kw):
    return pltpu.CompilerParams(vmem_limit_bytes=VMEM_LIMIT, **kw)


def _att_tables():
    rq = np.arange(2 * GRID_W) % GRID_W
    kc = np.arange(KWIN) % GRID_W
    win_start = np.clip(rq - WIN_W // 2, 0, GRID_W - WIN_W)
    valid = (kc[None, :] >= win_start[:, None]) & (kc[None, :] < win_start[:, None] + WIN_W)
    return valid.astype(np.float32), _pair_mask()


def _pair_mask():
    half = np.arange(2 * DH) // DH
    return (half[:, None] == half[None, :]).astype(np.float32)


def _dup_table():
    return np.concatenate([np.eye(REC_BLOCK, dtype=np.float32)] * 2, axis=1)


def _sigmoid(x):
    return 0.5 * jnp.tanh(0.5 * x) + 0.5


def _softplus(x):
    return jnp.maximum(x, 0.0) + jnp.log(1.0 + jnp.exp(-jnp.abs(x)))


def _one_minus_square(log_a, a):
    x = 2.0 * log_a
    series = -x * (1.0 + x * (0.5 + x * (1.0 / 6.0)))
    return jnp.where(x > -0.02, series, 1.0 - a * a)


_GELU_C = math.sqrt(2.0 / math.pi)


def _gelu_and_grad(x):
    x2 = x * x
    inner = _GELU_C * (x + 0.044715 * x * x2)
    t = jnp.tanh(inner)
    g = 0.5 * x * (1.0 + t)
    dg = 0.5 * (1.0 + t) + 0.5 * x * (1.0 - t * t) * _GELU_C * (1.0 + 3.0 * 0.044715 * x2)
    return g, dg


def _dot(a, b):
    return jnp.dot(a, b, preferred_element_type=F32)


def _dot_nt(a, b):
    return lax.dot_general(a, b, (((1,), (1,)), ((), ())), preferred_element_type=F32)


def _dot_tn(a, b):
    return lax.dot_general(a, b, (((0,), (0,)), ((), ())), preferred_element_type=F32)


def _dot_exact(a, b):
    return jnp.dot(a, b, precision=lax.Precision.HIGHEST, preferred_element_type=F32)


def _shift_rows(x, s):
    n = x.shape[0]
    rows = lax.broadcasted_iota(jnp.int32, x.shape, 0)
    y = pltpu.roll(x, s % n, 0)
    if s > 0:
        return jnp.where(rows >= s, y, 0.0)
    return jnp.where(rows < n + s, y, 0.0)


def _rms_bwd(dh, xh, r, g):
    dxh = dh * g
    return r * (dxh - xh * jnp.mean(dxh * xh, axis=-1, keepdims=True))


def _matmul(a, b, mode, out_dtype, name, tm=512, tn=1024, tk=2048):
    if mode == "nn":
        (m, k), (k2, n) = a.shape, b.shape
    elif mode == "nt":
        (m, k), (n, k2) = a.shape, b.shape
    else:
        (k, m), (k2, n) = a.shape, b.shape
    assert k == k2
    tm, tn, tk = min(tm, m), min(tn, n), min(tk, k)
    assert m % tm == 0 and n % tn == 0 and k % tk == 0
    nk = k // tk
    dot = {"nn": _dot, "nt": _dot_nt, "tn": _dot_tn}[mode]

    def body(a_ref, b_ref, o_ref, acc):
        kk = pl.program_id(2)
        part = dot(a_ref[...].astype(BF16), b_ref[...].astype(BF16))
        if nk == 1:
            o_ref[...] = part.astype(out_dtype)
            return

        @pl.when(kk == 0)
        def _():
            acc[...] = part

        @pl.when(kk > 0)
        def _():
            acc[...] += part

        @pl.when(kk == nk - 1)
        def _():
            o_ref[...] = acc[...].astype(out_dtype)

    if mode == "tn":
        a_spec = pl.BlockSpec((tk, tm), lambda i, j, kk: (kk, i))
    else:
        a_spec = pl.BlockSpec((tm, tk), lambda i, j, kk: (i, kk))
    if mode == "nt":
        b_spec = pl.BlockSpec((tn, tk), lambda i, j, kk: (j, kk))
    else:
        b_spec = pl.BlockSpec((tk, tn), lambda i, j, kk: (kk, j))
    return pl.pallas_call(
        body, name=name,
        out_shape=jax.ShapeDtypeStruct((m, n), out_dtype),
        grid=(m // tm, n // tn, nk),
        in_specs=[a_spec, b_spec],
        out_specs=pl.BlockSpec((tm, tn), lambda i, j, kk: (i, j)),
        scratch_shapes=[pltpu.VMEM((tm, tn) if nk > 1 else (8, LANES), F32)],
        compiler_params=_params(dimension_semantics=("parallel", "parallel", "arbitrary")),
    )(a, b)


def _in_proj(x, g1, w_in_t, b_in):
    tm = 1024

    def body(x_ref, g_ref, w_ref, b_ref, qkv_ref, uy_ref, gg_ref, h_ref, h_scr):
        j = pl.program_id(1)

        @pl.when(j == 0)
        def _():
            xv = x_ref[...]
            r = lax.rsqrt(jnp.mean(xv * xv, axis=-1, keepdims=True) + EPS)
            h = ((xv * r) * g_ref[...]).astype(BF16)
            h_scr[...] = h
            h_ref[...] = h

        z = _dot_nt(h_scr[...], w_ref[...]) + b_ref[...]

        @pl.when(j < 3)
        def _():
            qkv_ref[...] = z.astype(BF16)

        @pl.when((j >= 3) & (j < 7))
        def _():
            uy_ref[...] = z

        @pl.when(j >= 7)
        def _():
            gg_ref[...] = z

    return pl.pallas_call(
        body, name="in_proj",
        out_shape=(jax.ShapeDtypeStruct((T, 3 * D_ATT), BF16),
                   jax.ShapeDtypeStruct((T, 2 * D_REC), F32),
                   jax.ShapeDtypeStruct((T, 2 * D), F32),
                   jax.ShapeDtypeStruct((T, D), BF16)),
        grid=(T // tm, N_DZ_TILES),
        in_specs=[pl.BlockSpec((tm, D), lambda i, j: (i, 0)),
                  pl.BlockSpec((1, D), lambda i, j: (0, 0)),
                  pl.BlockSpec((TILE, D), lambda i, j: (j, 0)),
                  pl.BlockSpec((1, TILE), lambda i, j: (0, j))],
        out_specs=(pl.BlockSpec((tm, TILE), lambda i, j: (i, jnp.minimum(j, 2))),
                   pl.BlockSpec((tm, TILE), lambda i, j: (i, jnp.clip(j - 3, 0, 3))),
                   pl.BlockSpec((tm, TILE), lambda i, j: (i, jnp.clip(j - 7, 0, 3))),
                   pl.BlockSpec((tm, D), lambda i, j: (i, 0))),
        scratch_shapes=[pltpu.VMEM((tm, D), BF16)],
        compiler_params=_params(dimension_semantics=("parallel", "arbitrary")),
    )(x, g1, w_in_t, b_in)


def _dz_specs(rows, tile_of, row_of):
    def spec(off, n, per_plane):
        def index(*ids):
            t = jnp.clip(tile_of(*ids) - off, 0, n - 1)
            return (t // per_plane, row_of(*ids), t % per_plane)
        return pl.BlockSpec((1, rows, TILE), index)
    return [spec(off, n, per) for off, n, per in DZ_ARRAYS]


def _dh_norm1_bwd(dz, w_in_t, x, g1, dx1):
    tm = 1024

    def body(*refs):
        seg_refs = refs[:3]
        w_ref, x_ref, g_ref, dx1_ref, gx_ref, dg_ref, acc = refs[3:]
        i, kk = pl.program_id(0), pl.program_id(1)

        @pl.when(kk == 0)
        def _():
            acc[...] = jnp.zeros_like(acc)

        for s, (off, n, _) in enumerate(DZ_ARRAYS):
            @pl.when((kk >= off) & (kk < off + n))
            def _(s=s):
                acc[...] += _dot(seg_refs[s][0], w_ref[...])

        @pl.when((i == 0) & (kk == 0))
        def _():
            dg_ref[...] = jnp.zeros_like(dg_ref)

        @pl.when(kk == N_DZ_TILES - 1)
        def _():
            xv = x_ref[...]
            r = lax.rsqrt(jnp.mean(xv * xv, axis=-1, keepdims=True) + EPS)
            xh = xv * r
            dh = acc[...]
            dg_ref[...] += jnp.sum(dh * xh, axis=0, keepdims=True)
            gx_ref[...] = dx1_ref[...] + _rms_bwd(dh, xh, r, g_ref[...])

    tok = pl.BlockSpec((tm, D), lambda i, j: (i, 0))
    vec = pl.BlockSpec((1, D), lambda i, j: (0, 0))
    return pl.pallas_call(
        body, name="dh_norm1_bwd",
        out_shape=(jax.ShapeDtypeStruct((T, D), F32), jax.ShapeDtypeStruct((1, D), F32)),
        grid=(T // tm, N_DZ_TILES),
        in_specs=_dz_specs(tm, lambda i, j: j, lambda i, j: i)
        + [pl.BlockSpec((TILE, D), lambda i, j: (j, 0)), tok, vec, tok],
        out_specs=(tok, vec),
        scratch_shapes=[pltpu.VMEM((tm, D), F32)],
        compiler_params=_params(dimension_semantics=("arbitrary", "arbitrary")),
    )(*dz, w_in_t, x, g1, dx1)


def _grad_w_in(dz, h):
    def body(*refs):
        seg_refs = refs[:3]
        h_ref, gw_ref, gb_ref = refs[3:]
        j = pl.program_id(0)

        for s, (off, n, _) in enumerate(DZ_ARRAYS):
            @pl.when((j >= off) & (j < off + n))
            def _(s=s):
                a = seg_refs[s][0]
                gw_ref[...] = _dot_tn(a, h_ref[...]).astype(BF16)
                gb_ref[...] = jnp.sum(a.astype(F32), axis=0, keepdims=True)

    return pl.pallas_call(
        body, name="grad_w_in",
        out_shape=(jax.ShapeDtypeStruct((D_IN, D), BF16), jax.ShapeDtypeStruct((1, D_IN), F32)),
        grid=(N_DZ_TILES,),
        in_specs=_dz_specs(T, lambda j: j, lambda j: 0) + [pl.BlockSpec((T, D), lambda j: (0, 0))],
        out_specs=(pl.BlockSpec((TILE, D), lambda j: (j, 0)), pl.BlockSpec((1, TILE), lambda j: (0, j))),
        compiler_params=_params(dimension_semantics=("parallel",)),
    )(*dz, h)


def _rpb_rows(rpb):
    padded = jnp.pad(rpb, ((0, 0), (0, 0), (0, GRID_W - N_RPB_C)))
    rows = [padded[:, WIN_H - 1 - oi: 2 * WIN_H - 1 - oi].reshape(N_HEADS // 2, 2, KWIN)
            for oi in range(WIN_H)]
    return jnp.stack(rows, axis=0)


SKEW = KWIN - (WIN_W - 1)


def _bias_tiles(rows_ref, bias_s):
    for oi in range(WIN_H):
        for hh in range(2):
            row = jnp.broadcast_to(rows_ref[oi, 0, hh:hh + 1, :], (GRID_W, KWIN))
            bias_s[oi, hh * GRID_W:(hh + 1) * GRID_W, :] = pltpu.roll(row, SKEW, 1, stride=1, stride_axis=0)


def _bias_tile_grads(gb_s, flip, out_ref):
    for oi in range(WIN_H):
        for hh in range(2):
            g = _dot_exact(flip, gb_s[oi, hh * GRID_W:(hh + 1) * GRID_W, :])
            back = pltpu.roll(g, KWIN - (GRID_W - WIN_W), 1, stride=1, stride_axis=0)
            out_ref[0, oi, hh:hh + 1, :] = jnp.sum(back, axis=0, keepdims=True)


def _rpb_fold(row_grads):
    g = row_grads.transpose(1, 0, 2, 3).reshape(WIN_H, N_HEADS, WIN_H, GRID_W)
    g = g.transpose(0, 2, 1, 3)

    def body(g_ref, o_ref):
        for dr in range(N_RPB_R):
            terms = [g_ref[oi, i] for oi in range(WIN_H) for i in range(WIN_H) if i - oi + WIN_H - 1 == dr]
            acc = terms[0]
            for term in terms[1:]:
                acc = acc + term
            o_ref[dr] = acc

    out = pl.pallas_call(
        body, name="rpb_fold",
        out_shape=jax.ShapeDtypeStruct((N_RPB_R, N_HEADS, GRID_W), F32),
    )(g)
    return out.transpose(1, 0, 2)[:, :, :N_RPB_C]


def _att_scores(q_ref, k_ref, bias_ref, valid, hmask, r):
    rs = jnp.clip(r - WIN_H // 2, 0, ROWS - WIN_H)
    oi = r - rs
    q0 = pl.multiple_of(r * GRID_W, GRID_W)
    k0 = pl.multiple_of(rs * GRID_W, GRID_W)
    q_r = q_ref[pl.ds(q0, GRID_W), :]
    q2 = jnp.where(hmask, jnp.concatenate([q_r, q_r], axis=0), jnp.zeros((), BF16))
    kw = k_ref[pl.ds(k0, KWIN), :]
    s = _dot_nt(q2, kw) * (DH ** -0.5) + bias_ref[oi]
    s = jnp.where(valid, s, -1e30)
    m = jnp.max(s, axis=-1, keepdims=True)
    p = jnp.exp(s - m)
    p = p / jnp.sum(p, axis=-1, keepdims=True)
    return p, q2, kw, q0, k0, oi


def _att_fwd(qkv, bias_rows):
    valid_np, hmask_np = _att_tables()

    def body(q_ref, k_ref, v_ref, rows_ref, valid_ref, hmask_ref, o_ref, bias_s):
        valid = valid_ref[...] > 0.5
        hmask = hmask_ref[...] > 0.5
        first_head = lax.broadcasted_iota(jnp.int32, (GRID_W, 2 * DH), 1) < DH
        _bias_tiles(rows_ref, bias_s)

        def row(r, carry):
            p, _, _, q0, k0, _ = _att_scores(q_ref, k_ref, bias_s, valid, hmask, r)
            o2 = _dot(p.astype(BF16), v_ref[pl.ds(k0, KWIN), :])
            o_ref[pl.ds(q0, GRID_W), :] = jnp.where(first_head, o2[:GRID_W], o2[GRID_W:]).astype(BF16)
            return carry

        lax.fori_loop(0, ROWS, row, 0, unroll=4)

    col = lambda off: pl.BlockSpec((T, 2 * DH), lambda hp: (0, hp + off))
    return pl.pallas_call(
        body, name="att_fwd",
        out_shape=jax.ShapeDtypeStruct((T, D_ATT), BF16),
        grid=(N_HEADS // 2,),
        in_specs=[col(0), col(4), col(8),
                  pl.BlockSpec((WIN_H, 1, 2, KWIN), lambda hp: (0, hp, 0, 0)),
                  pl.BlockSpec((2 * GRID_W, KWIN), lambda hp: (0, 0)),
                  pl.BlockSpec((2 * DH, 2 * DH), lambda hp: (0, 0))],
        out_specs=pl.BlockSpec((T, 2 * DH), lambda hp: (0, hp)),
        scratch_shapes=[pltpu.VMEM((WIN_H, 2 * GRID_W, KWIN), F32)],
        compiler_params=_params(dimension_semantics=("parallel",)),
    )(qkv, qkv, qkv, bias_rows, jnp.asarray(valid_np), jnp.asarray(hmask_np))


def _att_bwd(qkv, bias_rows, datt, after):
    valid_np, hmask_np = _att_tables()

    def body(q_ref, k_ref, v_ref, do_ref, rows_ref, valid_ref, hmask_ref, flip_ref,
             dqkv_ref, grows_ref, dk_acc, dv_acc, bias_s, gb_s):
        valid = valid_ref[...] > 0.5
        hmask = hmask_ref[...] > 0.5
        first_head = lax.broadcasted_iota(jnp.int32, (GRID_W, 2 * DH), 1) < DH
        dk_acc[...] = jnp.zeros_like(dk_acc)
        dv_acc[...] = jnp.zeros_like(dv_acc)
        gb_s[...] = jnp.zeros_like(gb_s)
        _bias_tiles(rows_ref, bias_s)

        def row(r, carry):
            p, q2, kw, q0, k0, oi = _att_scores(q_ref, k_ref, bias_s, valid, hmask, r)
            do_r = do_ref[pl.ds(q0, GRID_W), :]
            do2 = jnp.where(hmask, jnp.concatenate([do_r, do_r], axis=0), jnp.zeros((), BF16))
            vw = v_ref[pl.ds(k0, KWIN), :]
            dp = _dot_nt(do2, vw)
            ds = p * (dp - jnp.sum(dp * p, axis=-1, keepdims=True))
            p16 = p.astype(BF16)
            ds16 = ds.astype(BF16)
            dv_acc[pl.ds(k0, KWIN), :] += _dot_tn(p16, do2)
            dk_acc[pl.ds(k0, KWIN), :] += _dot_tn(ds16, q2) * (DH ** -0.5)
            dq2 = _dot(ds16, kw) * (DH ** -0.5)
            dqkv_ref[0, pl.ds(q0, GRID_W), :] = jnp.where(first_head, dq2[:GRID_W], dq2[GRID_W:]).astype(BF16)
            gb_s[oi] += ds
            return carry

        lax.fori_loop(0, ROWS, row, 0, unroll=4)
        dqkv_ref[1] = dk_acc[...].astype(BF16)
        dqkv_ref[2] = dv_acc[...].astype(BF16)
        _bias_tile_grads(gb_s, flip_ref[...], grows_ref)

    col = lambda off: pl.BlockSpec((T, 2 * DH), lambda hp: (0, hp + off))
    tiles = pltpu.VMEM((WIN_H, 2 * GRID_W, KWIN), F32)
    return pl.pallas_call(
        body, name="att_bwd",
        out_shape=(jax.ShapeDtypeStruct((3, T, D_ATT), BF16),
                   jax.ShapeDtypeStruct((N_HEADS // 2, WIN_H, 2, KWIN), F32)),
        grid=(N_HEADS // 2,),
        in_specs=[col(0), col(4), col(8), col(0),
                  pl.BlockSpec((WIN_H, 1, 2, KWIN), lambda hp: (0, hp, 0, 0)),
                  pl.BlockSpec((2 * GRID_W, KWIN), lambda hp: (0, 0)),
                  pl.BlockSpec((2 * DH, 2 * DH), lambda hp: (0, 0)),
                  pl.BlockSpec((GRID_W, GRID_W), lambda hp: (0, 0))],
        out_specs=(pl.BlockSpec((3, T, 2 * DH), lambda hp: (0, 0, hp)),
                   pl.BlockSpec((1, WIN_H, 2, KWIN), lambda hp: (hp, 0, 0, 0))),
        scratch_shapes=[pltpu.VMEM((T, 2 * DH), F32), pltpu.VMEM((T, 2 * DH), F32), tiles, tiles],
        compiler_params=_params(dimension_semantics=("parallel",)),
    )(qkv, qkv, qkv, datt, bias_rows, jnp.asarray(valid_np) + after, jnp.asarray(hmask_np),
      jnp.asarray(np.eye(GRID_W, dtype=np.float32)[::-1].copy()))


def _conv_taps(up):
    return (_shift_rows(up, 2), _shift_rows(up, 1), up, _shift_rows(up, -1))


def _pair_block_diag(w_pair, dup, same_half):
    return jnp.where(same_half, _dot(w_pair.astype(BF16), dup), 0.0).astype(BF16)


def _gates(u, u16, wa, ba, wi, bi, lam):
    r = _sigmoid(_dot(u16, wa) + ba)
    ig = _sigmoid(_dot(u16, wi) + bi)
    sp = _softplus(-lam)
    log_a = (-LRU_C) * r * sp
    a = jnp.exp(log_a)
    mult = jnp.sqrt(jnp.maximum(_one_minus_square(log_a, a), 0.0))
    return r, ig, sp, a, mult


SCAN_BLOCKS = 2


def _scans(jobs):
    c = jobs[0][0].shape[1]
    nblk = T // 8
    rows = lax.broadcasted_iota(jnp.int32, (8, c), 0)

    def block(a, b, reverse):
        for s in (1, 2, 4):
            if reverse:
                keep = rows < 8 - s
                a_s = jnp.where(keep, pltpu.roll(a, 8 - s, 0), 1.0)
                b_s = jnp.where(keep, pltpu.roll(b, 8 - s, 0), 0.0)
            else:
                keep = rows >= s
                a_s = jnp.where(keep, pltpu.roll(a, s, 0), 1.0)
                b_s = jnp.where(keep, pltpu.roll(b, s, 0), 0.0)
            b = a * b_s + b
            a = a * a_s
        return a, b

    def step(i, carry):
        out = []
        for (a_ref, b_ref, h_ref, reverse), h_prev in zip(jobs, carry):
            for u in range(SCAN_BLOCKS):
                blk = i * SCAN_BLOCKS + u
                if reverse:
                    blk = nblk - 1 - blk
                t0 = pl.multiple_of(blk * 8, 8)
                a, b = block(a_ref[pl.ds(t0, 8), :], b_ref[pl.ds(t0, 8), :], reverse)
                h = a * h_prev + b
                h_ref[pl.ds(t0, 8), :] = h
                h_prev = jnp.broadcast_to(h[0:1] if reverse else h[7:8], (8, c))
            out.append(h_prev)
        return tuple(out)

    lax.fori_loop(0, nblk // SCAN_BLOCKS, step, tuple(jnp.zeros((8, c), F32) for _ in jobs))


def _rec_specs():
    tok = lambda off: pl.BlockSpec((T, CG), lambda g: (0, g + off))
    per_ch = lambda rows: pl.BlockSpec((rows, CG), lambda g: (0, g))
    wspec = pl.BlockSpec((2, 1, CG, REC_BLOCK), lambda g: (0, g, 0, 0))
    const = lambda shape: pl.BlockSpec(shape, lambda g: (0, 0))
    return tok, per_ch, wspec, const


def _rec_fwd(uy, conv_w, conv_b, w_a, b_a, w_i, b_i, lam):
    tok, per_ch, wspec, const = _rec_specs()

    def body(up_ref, yb_ref, cw_ref, cb_ref, wa_ref, ba_ref, wi_ref, bi_ref, lam_ref, dup_ref, half_ref,
             hf_ref, hb_ref, yrec_ref, a_f, bx_f, a_b, bx_b):
        dup = dup_ref[...]
        same_half = half_ref[...] > 0.5
        taps = _conv_taps(up_ref[...])
        u = cb_ref[...]
        for j in range(4):
            u = u + taps[j] * cw_ref[j:j + 1, :]
        u16 = u.astype(BF16)
        for d, (a_s, bx_s) in enumerate(((a_f, bx_f), (a_b, bx_b))):
            wa = _pair_block_diag(wa_ref[d, 0], dup, same_half)
            wi = _pair_block_diag(wi_ref[d, 0], dup, same_half)
            _, ig, _, a, mult = _gates(u, u16, wa, ba_ref[d:d + 1, :], wi, bi_ref[d:d + 1, :],
                                       lam_ref[d:d + 1, :])
            a_s[...] = a
            bx_s[...] = mult * (ig * u)
        _scans([(a_f, bx_f, hf_ref, False), (a_b, bx_b, hb_ref, True)])
        gelu, _ = _gelu_and_grad(yb_ref[...])
        yrec_ref[...] = ((hf_ref[...] + hb_ref[...]) * gelu).astype(BF16)

    return pl.pallas_call(
        body, name="rec_fwd",
        out_shape=(jax.ShapeDtypeStruct((T, D_REC), F32), jax.ShapeDtypeStruct((T, D_REC), F32),
                   jax.ShapeDtypeStruct((T, D_REC), BF16)),
        grid=(N_CG,),
        in_specs=[tok(0), tok(N_CG), per_ch(4), per_ch(1), wspec, per_ch(2), wspec, per_ch(2), per_ch(2),
                  const((REC_BLOCK, CG)), const((CG, CG))],
        out_specs=(tok(0), tok(0), tok(0)),
        scratch_shapes=[pltpu.VMEM((T, CG), F32)] * 4,
        compiler_params=_params(dimension_semantics=("parallel",)),
    )(uy, uy, conv_w, conv_b, w_a, b_a, w_i, b_i, lam,
      jnp.asarray(_dup_table(), BF16), jnp.asarray(_pair_mask()))


def _rec_bwd(uy, hf, hb, dyrec, conv_w, conv_b, w_a, b_a, w_i, b_i, lam):
    tok, per_ch, wspec, const = _rec_specs()

    def body(up_ref, yb_ref, hf_ref, hb_ref, dy_ref, cw_ref, cb_ref, wa_ref, ba_ref, wi_ref, bi_ref,
             lam_ref, dup_ref, dupt_ref, half_ref,
             duy_ref, dcw_ref, dcb_ref, dwa_ref, dba_ref, dwi_ref, dbi_ref, dlam_ref,
             a_s0, a_s1, dh_s, g_s0, g_s1):
        dup = dup_ref[...]
        dup_t = dupt_ref[...]
        same_half = half_ref[...] > 0.5
        taps = _conv_taps(up_ref[...])
        u = cb_ref[...]
        for j in range(4):
            u = u + taps[j] * cw_ref[j:j + 1, :]
        u16 = u.astype(BF16)
        gelu, dgelu = _gelu_and_grad(yb_ref[...])
        dy = dy_ref[...]
        duy_ref[1] = (dy * (hf_ref[...] + hb_ref[...]) * dgelu).astype(BF16)
        dh_s[...] = dy * gelu
        gate_values = []
        for d, a_s in enumerate((a_s0, a_s1)):
            wa = _pair_block_diag(wa_ref[d, 0], dup, same_half)
            wi = _pair_block_diag(wi_ref[d, 0], dup, same_half)
            lam_d = lam_ref[d:d + 1, :]
            r, ig, sp, a, mult = _gates(u, u16, wa, ba_ref[d:d + 1, :], wi, bi_ref[d:d + 1, :], lam_d)
            a_s[...] = _shift_rows(a, 1 if d == 1 else -1)
            gate_values.append((wa, wi, lam_d, r, ig, sp, a, mult))
        _scans([(a_s0, dh_s, g_s0, True), (a_s1, dh_s, g_s1, False)])
        du = jnp.zeros((T, CG), F32)
        for d, g_s in enumerate((g_s0, g_s1)):
            reverse = d == 1
            wa, wi, lam_d, r, ig, sp, a, mult = gate_values[d]
            g = g_s[...]
            h_prev = _shift_rows(hb_ref[...], -1) if reverse else _shift_rows(hf_ref[...], 1)
            da = g * h_prev
            dmult = g * (ig * u)
            dig = g * mult * u
            du = du + g * mult * ig
            dmult_dlog = jnp.where(mult > 0.0, -(a * a) / mult, 0.0)
            dlog_a = da * a + dmult * dmult_dlog
            dr = dlog_a * ((-LRU_C) * sp)
            dsp = jnp.sum(dlog_a * ((-LRU_C) * r), axis=0, keepdims=True)
            dlam_ref[d:d + 1, :] = dsp * (-_sigmoid(-lam_d))
            dga = dr * r * (1.0 - r)
            dgi = dig * ig * (1.0 - ig)
            dga16 = dga.astype(BF16)
            dgi16 = dgi.astype(BF16)
            du = du + _dot_nt(dga16, wa) + _dot_nt(dgi16, wi)
            dwa_ref[d, 0] = _dot_exact(jnp.where(same_half, _dot_tn(u16, dga16), 0.0), dup_t)
            dwi_ref[d, 0] = _dot_exact(jnp.where(same_half, _dot_tn(u16, dgi16), 0.0), dup_t)
            dba_ref[d:d + 1, :] = jnp.sum(dga, axis=0, keepdims=True)
            dbi_ref[d:d + 1, :] = jnp.sum(dgi, axis=0, keepdims=True)
        dcb_ref[...] = jnp.sum(du, axis=0, keepdims=True)
        for j in range(4):
            dcw_ref[j:j + 1, :] = jnp.sum(du * taps[j], axis=0, keepdims=True)
        dup_in = (_shift_rows(du, -2) * cw_ref[0:1, :] + _shift_rows(du, -1) * cw_ref[1:2, :]
                  + du * cw_ref[2:3, :] + _shift_rows(du, 1) * cw_ref[3:4, :])
        duy_ref[0] = dup_in.astype(BF16)

    wshape = jax.ShapeDtypeStruct((2, N_CG, CG, REC_BLOCK), F32)
    vec = lambda rows: jax.ShapeDtypeStruct((rows, D_REC), F32)
    dup_np = _dup_table()
    return pl.pallas_call(
        body, name="rec_bwd",
        out_shape=(jax.ShapeDtypeStruct((2, T, D_REC), BF16),
                   vec(4), vec(1), wshape, vec(2), wshape, vec(2), vec(2)),
        grid=(N_CG,),
        in_specs=[tok(0), tok(N_CG), tok(0), tok(0), tok(0),
                  per_ch(4), per_ch(1), wspec, per_ch(2), wspec, per_ch(2), per_ch(2),
                  const((REC_BLOCK, CG)), const((CG, REC_BLOCK)), const((CG, CG))],
        out_specs=(pl.BlockSpec((2, T, CG), lambda g: (0, 0, g)),
                   per_ch(4), per_ch(1), wspec, per_ch(2), wspec, per_ch(2), per_ch(2)),
        scratch_shapes=[pltpu.VMEM((T, CG), F32)] * 5,
        compiler_params=_params(dimension_semantics=("parallel",)),
    )(uy, uy, hf, hb, dyrec, conv_w, conv_b, w_a, b_a, w_i, b_i, lam,
      jnp.asarray(dup_np, BF16), jnp.asarray(dup_np.T.copy()), jnp.asarray(_pair_mask()))


TM_MIX = 256


def _mix_specs():
    tok = lambda width, blk=0: pl.BlockSpec((TM_MIX, width), lambda i: (i, blk))
    full = lambda shape: pl.BlockSpec(shape, lambda i: (0, 0))
    return tok, full


def _mix_fwd(x, att, yrec, gg, w_att_o_t, w_rec_o, w_out):
    tok, full = _mix_specs()

    def body(x_ref, att_ref, yr_ref, ga_ref, gr_ref, wao_ref, wro_ref, wo_ref, x1_ref, mixed_ref):
        y_att = _dot_nt(att_ref[...], wao_ref[...])
        y_rec = _dot(yr_ref[...], wro_ref[...])
        mixed = (_sigmoid(ga_ref[...]) * y_att + _sigmoid(gr_ref[...]) * y_rec).astype(BF16)
        mixed_ref[...] = mixed
        x1_ref[...] = x_ref[...] + _dot(mixed, wo_ref[...])

    return pl.pallas_call(
        body, name="mix_fwd",
        out_shape=(jax.ShapeDtypeStruct((T, D), F32), jax.ShapeDtypeStruct((T, D), BF16)),
        grid=(T // TM_MIX,),
        in_specs=[tok(D), tok(D_ATT), tok(D_REC), tok(D, 0), tok(D, 1),
                  full((D, D_ATT)), full((D_REC, D)), full((D, D))],
        out_specs=(tok(D), tok(D)),
        compiler_params=_params(dimension_semantics=("parallel",)),
    )(x, att, yrec, gg, gg, w_att_o_t, w_rec_o, w_out)


def _mix_bwd(dx1, att, yrec, gg, w_att_o_t, w_rec_o, w_out):
    tok, full = _mix_specs()

    def body(dx_ref, att_ref, yr_ref, ga_ref, gr_ref, wao_ref, wro_ref, wo_ref,
             dgg_ref, dya_ref, dyr_ref, datt_ref, dyrp_ref):
        dmixed = _dot_nt(dx_ref[...].astype(BF16), wo_ref[...])
        y_att = _dot_nt(att_ref[...], wao_ref[...])
        y_rec = _dot(yr_ref[...], wro_ref[...])
        sa = _sigmoid(ga_ref[...])
        sr = _sigmoid(gr_ref[...])
        dgg_ref[0] = (dmixed * y_att * sa * (1.0 - sa)).astype(BF16)
        dgg_ref[1] = (dmixed * y_rec * sr * (1.0 - sr)).astype(BF16)
        dya = (dmixed * sa).astype(BF16)
        dyr = (dmixed * sr).astype(BF16)
        dya_ref[...] = dya
        dyr_ref[...] = dyr
        datt_ref[...] = _dot(dya, wao_ref[...]).astype(BF16)
        dyrp_ref[...] = _dot_nt(dyr, wro_ref[...])

    return pl.pallas_call(
        body, name="mix_bwd",
        out_shape=(jax.ShapeDtypeStruct((2, T, D), BF16),
                   jax.ShapeDtypeStruct((T, D), BF16), jax.ShapeDtypeStruct((T, D), BF16),
                   jax.ShapeDtypeStruct((T, D_ATT), BF16), jax.ShapeDtypeStruct((T, D_REC), F32)),
        grid=(T // TM_MIX,),
        in_specs=[tok(D), tok(D_ATT), tok(D_REC), tok(D, 0), tok(D, 1),
                  full((D, D_ATT)), full((D_REC, D)), full((D, D))],
        out_specs=(pl.BlockSpec((2, TM_MIX, D), lambda i: (0, i, 0)),
                   tok(D), tok(D), tok(D_ATT), tok(D_REC)),
        compiler_params=_params(dimension_semantics=("parallel",)),
    )(dx1, att, yrec, gg, gg, w_att_o_t, w_rec_o, w_out)


TM_FFN = 256
FF_CHUNK = 1024


def _ffn_loss(x1, target, g2, gf, w_ff1_t, w_ff2):
    n_chunks = D_FF // FF_CHUNK

    def body(x1_ref, tg_ref, g2_ref, gf_ref, w1_hbm, w2_hbm,
             loss_ref, dx1_ref, h2_ref, act_ref, dpre_ref, dx2_ref, dg2_ref, dgf_ref,
             w1, w2, relu_s):
        i = pl.program_id(0)

        @pl.when(i == 0)
        def _():
            pltpu.sync_copy(w1_hbm, w1)
            pltpu.sync_copy(w2_hbm, w2)
            loss_ref[...] = jnp.zeros_like(loss_ref)
            dg2_ref[...] = jnp.zeros_like(dg2_ref)
            dgf_ref[...] = jnp.zeros_like(dgf_ref)

        x1v = x1_ref[...]
        r2 = lax.rsqrt(jnp.mean(x1v * x1v, axis=-1, keepdims=True) + EPS)
        xh2 = x1v * r2
        h2 = (xh2 * g2_ref[...]).astype(BF16)
        h2_ref[...] = h2
        x2 = x1v
        for c in range(n_chunks):
            ff = slice(c * FF_CHUNK, (c + 1) * FF_CHUNK)
            rl = jnp.maximum(_dot_nt(h2, w1[ff, :]), 0.0)
            relu_s[:, ff] = rl
            act = (rl * rl).astype(BF16)
            act_ref[:, ff] = act
            x2 = x2 + _dot(act, w2[ff, :])
        r3 = lax.rsqrt(jnp.mean(x2 * x2, axis=-1, keepdims=True) + EPS)
        xh3 = x2 * r3
        err = xh3 * gf_ref[...] - tg_ref[...]
        loss_ref[...] += 0.5 * jnp.sum(jnp.mean(err * err, axis=-1, keepdims=True))
        dy = err * (1.0 / D)
        dgf_ref[...] += jnp.sum(dy * xh3, axis=0, keepdims=True)
        dx2 = _rms_bwd(dy, xh3, r3, gf_ref[...])
        dx2_16 = dx2.astype(BF16)
        dx2_ref[...] = dx2_16
        dh2 = jnp.zeros((TM_FFN, D), F32)
        for c in range(n_chunks):
            ff = slice(c * FF_CHUNK, (c + 1) * FF_CHUNK)
            dpre = (_dot_nt(dx2_16, w2[ff, :]) * (2.0 * relu_s[:, ff])).astype(BF16)
            dpre_ref[:, ff] = dpre
            dh2 = dh2 + _dot(dpre, w1[ff, :])
        dg2_ref[...] += jnp.sum(dh2 * xh2, axis=0, keepdims=True)
        dx1_ref[...] = dx2 + _rms_bwd(dh2, xh2, r2, g2_ref[...])

    tok = lambda width: pl.BlockSpec((TM_FFN, width), lambda i: (i, 0))
    vec = pl.BlockSpec((1, D), lambda i: (0, 0))
    hbm = pl.BlockSpec(memory_space=pl.ANY)
    return pl.pallas_call(
        body, name="ffn_loss",
        out_shape=(jax.ShapeDtypeStruct((8, 128), F32), jax.ShapeDtypeStruct((T, D), F32),
                   jax.ShapeDtypeStruct((T, D), BF16), jax.ShapeDtypeStruct((T, D_FF), BF16),
                   jax.ShapeDtypeStruct((T, D_FF), BF16), jax.ShapeDtypeStruct((T, D), BF16),
                   jax.ShapeDtypeStruct((1, D), F32), jax.ShapeDtypeStruct((1, D), F32)),
        grid=(T // TM_FFN,),
        in_specs=[tok(D), tok(D), vec, vec, hbm, hbm],
        out_specs=(pl.BlockSpec((8, 128), lambda i: (0, 0)), tok(D), tok(D), tok(D_FF), tok(D_FF), tok(D),
                   vec, vec),
        scratch_shapes=[pltpu.VMEM((D_FF, D), BF16), pltpu.VMEM((D_FF, D), BF16),
                        pltpu.VMEM((TM_FFN, D_FF), F32)],
        compiler_params=_params(dimension_semantics=("arbitrary",)),
    )(x1, target, g2, gf, w_ff1_t, w_ff2)


def _local_step(x, target, p, late_weights, reduce_early):
    bias = _rpb_rows(p["rpb"])
    pairs = lambda w: w.reshape(2, N_CG, CG, REC_BLOCK)
    w_a, w_i = pairs(p["w_rg_a"]), pairs(p["w_rg_i"])
    rec_params = (p["conv_w"], p["conv_b"], w_a, p["b_rg_a"], w_i, p["b_rg_i"], p["lru_lambda"])

    qkv, uy, gg, h = _in_proj(x, p["ln1_g"], p["w_in_t"], p["b_in"])
    att = _att_fwd(qkv, bias)
    hf, hb, yrec = _rec_fwd(uy, *rec_params)
    p = {**p, **late_weights(yrec)}
    x1, mixed = _mix_fwd(x, att, yrec, gg, p["w_att_o_t"], p["w_rec_o"], p["w_out"])
    loss8, dx1, h2, act, dpre, dx2, g_ln2, g_lnf = _ffn_loss(
        x1, target, p["ln2_g"], p["lnf_g"], p["w_ff1_t"], p["w_ff2"])

    dgg, dya, dyr, datt, dyrp = _mix_bwd(dx1, att, yrec, gg, p["w_att_o_t"], p["w_rec_o"], p["w_out"])
    duy, g_cw, g_cb, g_wa, g_ba, g_wi, g_bi, g_lam = _rec_bwd(uy, hf, hb, dyrp, *rec_params)
    blocks = lambda g: g.reshape(2, N_REC_BLOCKS, REC_BLOCK, REC_BLOCK)
    grads = {
        "w_att_o_t": _matmul(dya, att, "tn", BF16, "g_w_att_o"),
        "conv_w": g_cw, "conv_b": g_cb, "w_rg_a": blocks(g_wa), "b_rg_a": g_ba,
        "w_rg_i": blocks(g_wi), "b_rg_i": g_bi, "lru_lambda": g_lam,
        "w_rec_o": _matmul(yrec, dyr, "tn", BF16, "g_w_rec_o"),
        "w_out": _matmul(mixed, dx1, "tn", BF16, "g_w_out"),
        "ln2_g": g_ln2,
        "w_ff1_t": _matmul(dpre, h2, "tn", BF16, "g_w_ff1"),
        "w_ff2": _matmul(act, dx2, "tn", BF16, "g_w_ff2"),
        "lnf_g": g_lnf,
    }
    after = reduce_early(grads)
    dqkv, gbias = _att_bwd(qkv, bias, datt, after)
    dz = (dqkv, duy, dgg)
    grad_x, g_ln1 = _dh_norm1_bwd(dz, p["w_in_t"], x, p["ln1_g"], dx1)
    g_w_in_t, g_b_in = _grad_w_in(dz, h)
    grads.update(ln1_g=g_ln1, w_in_t=g_w_in_t, b_in=g_b_in, rpb=_rpb_fold(gbias))
    return loss8[0:1, 0:1], grad_x, grads


MESH_ID = pl.DeviceIdType.MESH
ANY = pl.BlockSpec(memory_space=pl.ANY)

CHAN_BLOCK_ROWS = 32
SECTIONS = (("w_in_t", 704, D), ("w_rec_o", 128, D), ("w_out", 128, D), ("w_ff1_t", 512, D),
            ("w_ff2", 512, D), ("chan", CHAN_BLOCK_ROWS, D), ("w_att_o_t", 128, D_ATT))
N_SEC = len(SECTIONS)
N_CHAN_ROWS = 10
CHAN = (("conv_w", 4), ("b_rg_a", 2), ("b_rg_i", 2), ("lru_lambda", 2))


def _position():
    return lax.axis_index("x"), lax.axis_index("y"), lax.axis_index("c")


def _other_chips(x, y):
    return [(1 - x, y), (x, 1 - y), (1 - x, 1 - y)]


def _block_of(ref, dev, rows):
    return ref.at[pl.ds(pl.multiple_of(dev * rows, 16), rows)]


def _all_gather(shards, name):
    ns = len(shards)

    def body(*refs):
        x_refs, out_refs, done_ref = refs[:ns], refs[ns:2 * ns], refs[2 * ns]
        send_sems, recv_sems, local_sems = refs[2 * ns + 1:]
        done_ref[0, 0] = 0.0
        x, y, c = _position()
        me, sibling = (x, y, c), (x, y, 1 - c)
        chips = _other_chips(x, y)

        def rows(s, px, py, pc):
            return _block_of(out_refs[s], 4 * px + 2 * py + pc, shards[s].shape[0])

        def copy(k, s, block, to, from_shard=False):
            return pltpu.make_async_remote_copy(
                src_ref=x_refs[s] if from_shard else rows(s, *block), dst_ref=rows(s, *block),
                send_sem=send_sems.at[k * ns + s], recv_sem=recv_sems.at[k * ns + s],
                device_id=to, device_id_type=MESH_ID)

        sections = range(ns)
        mine = [pltpu.make_async_copy(x_refs[s], rows(s, *me), local_sems.at[s]) for s in sections]
        first = [copy(0, s, me, sibling, True) for s in sections]
        first += [copy(1 + j, s, me, (*chip, c), True) for j, chip in enumerate(chips) for s in sections]
        for cp in mine + first:
            cp.start()
        passed = []
        for j, chip in enumerate(chips):
            for s in sections:
                copy(1 + j, s, (*chip, c), me).wait_recv()
                passed.append(copy(4 + j, s, (*chip, c), sibling))
                passed[-1].start()
        for s in sections:
            copy(0, s, sibling, me).wait_recv()
        for j, chip in enumerate(chips):
            for s in sections:
                copy(4 + j, s, (*chip, 1 - c), me).wait_recv()
        for cp in first + passed:
            cp.wait_send()
        for cp in mine:
            cp.wait()

    return pl.pallas_call(
        body, name=name,
        out_shape=tuple(jax.ShapeDtypeStruct((N_DEV * s.shape[0], s.shape[1]), s.dtype) for s in shards)
        + (jax.ShapeDtypeStruct((1, 1), F32),),
        in_specs=[ANY] * ns,
        out_specs=(ANY,) * ns + (pl.BlockSpec(memory_space=pltpu.SMEM),),
        scratch_shapes=[pltpu.SemaphoreType.DMA((7 * ns,)), pltpu.SemaphoreType.DMA((7 * ns,)),
                        pltpu.SemaphoreType.DMA((ns,))],
    )(*shards)


HBM = pl.BlockSpec(memory_space=pltpu.HBM)
SEM = pl.BlockSpec(memory_space=pltpu.SEMAPHORE)
EFFECT = pltpu.SideEffectType.DATAFLOW_SIDE_EFFECTING


def _in_hbm(a):
    return pltpu.with_memory_space_constraint(a, pltpu.HBM)


def _first_hop_copies(shards, x_refs, zones, send_sems, recv_sems):
    ns = len(shards)
    x, y, c = _position()
    targets = [(x, y, 1 - c)] + [(cx, cy, c) for cx, cy in _other_chips(x, y)]
    return [pltpu.make_async_remote_copy(
        src_ref=x_refs[s], dst_ref=_block_of(zones[s], 4 * x + 2 * y + c, shards[s].shape[0]),
        send_sem=send_sems.at[k * ns + s], recv_sem=recv_sems.at[k * ns + s],
        device_id=to, device_id_type=MESH_ID)
        for k, to in enumerate(targets) for s in range(ns)]


def _own_blocks_placed(shards, after):
    ns = len(shards)
    x, y, c = _position()
    me = jnp.reshape(4 * x + 2 * y + c, (1,)).astype(jnp.int32)
    shards = [*shards[:-1], shards[-1] + after.astype(shards[-1].dtype)]

    def body(me_ref, *refs):
        for s in range(ns):
            refs[ns + s][...] = refs[s][...]

    return pl.pallas_call(
        body, name="own_blocks_placed",
        out_shape=tuple(jax.ShapeDtypeStruct((N_DEV * s.shape[0], s.shape[1]), s.dtype) for s in shards),
        grid_spec=pltpu.PrefetchScalarGridSpec(
            num_scalar_prefetch=1, grid=(1,),
            in_specs=[pl.BlockSpec(s.shape, lambda i, me: (0, 0)) for s in shards],
            out_specs=tuple(pl.BlockSpec(s.shape, lambda i, me: (me[0], 0)) for s in shards)),
        compiler_params=_params(dimension_semantics=("arbitrary",)),
    )(me, *shards)


def _gather_start(shards, after, name):
    ns = len(shards)
    zones = _own_blocks_placed(shards, after)

    def body(*refs):
        for cp in _first_hop_copies(shards, refs[:ns], refs[ns:2 * ns], refs[2 * ns], refs[2 * ns + 1]):
            cp.start()
        refs[-1][...] = jnp.zeros_like(refs[-1])

    out = pl.pallas_call(
        body, name=name,
        out_shape=(pltpu.SemaphoreType.DMA((4 * ns,)), pltpu.SemaphoreType.DMA((4 * ns,)),
                   *[pltpu.HBM(a.shape, a.dtype) for a in (*shards, *zones)],
                   jax.ShapeDtypeStruct((8, LANES), F32)),
        in_specs=[HBM] * (2 * ns),
        out_specs=(SEM, SEM, *[HBM] * (2 * ns), pl.BlockSpec(memory_space=pltpu.VMEM)),
        input_output_aliases={i: 2 + i for i in range(2 * ns)},
        compiler_params=pltpu.CompilerParams(has_side_effects=EFFECT),
    )(*[_in_hbm(a) for a in shards], *[_in_hbm(a) for a in zones])
    return out[0], out[1], out[2:2 + ns], out[2 + ns:2 + 2 * ns], out[-1]


def _gather_wait(send_sems, recv_sems, shards, zones, after, name):
    ns = len(shards)

    def body(*refs):
        for cp in _first_hop_copies(shards, refs[:ns], refs[ns:2 * ns], refs[2 * ns], refs[2 * ns + 1]):
            cp.wait_send()
            cp.wait_recv()

    out = pl.pallas_call(
        body, name=name,
        out_shape=tuple(pltpu.HBM(a.shape, a.dtype) for a in (*shards, *zones)),
        in_specs=[HBM] * (2 * ns) + [SEM, SEM, ANY],
        out_specs=(HBM,) * (2 * ns),
        input_output_aliases={i: i for i in range(2 * ns)},
        compiler_params=pltpu.CompilerParams(has_side_effects=EFFECT),
    )(*shards, *zones, send_sems, recv_sems, after)
    return out[ns:]


def _gather_pass_on(rows, zones, name):
    ns = len(zones)

    def body(*refs):
        in_refs, out_refs = refs[:ns], refs[ns:2 * ns]
        send_sems, recv_sems = refs[2 * ns:]
        x, y, c = _position()
        copies = [pltpu.make_async_remote_copy(
            src_ref=_block_of(in_refs[s], 4 * cx + 2 * cy + c, rows[s]),
            dst_ref=_block_of(out_refs[s], 4 * cx + 2 * cy + c, rows[s]),
            send_sem=send_sems.at[j * ns + s], recv_sem=recv_sems.at[j * ns + s],
            device_id=(x, y, 1 - c), device_id_type=MESH_ID)
            for j, (cx, cy) in enumerate(_other_chips(x, y)) for s in range(ns)]
        for cp in copies:
            cp.start()
        for cp in copies:
            cp.wait_recv()
        for cp in copies:
            cp.wait_send()

    return pl.pallas_call(
        body, name=name,
        out_shape=tuple(jax.ShapeDtypeStruct(z.shape, z.dtype) for z in zones),
        in_specs=[ANY] * ns, out_specs=(ANY,) * ns,
        input_output_aliases={i: i for i in range(ns)},
        scratch_shapes=[pltpu.SemaphoreType.DMA((3 * ns,)), pltpu.SemaphoreType.DMA((3 * ns,))],
    )(*zones)


def _pair_exchange(sections, grads, name):
    ns = len(sections)

    def body(*refs):
        g_refs, land = refs[:ns], refs[ns:2 * ns]
        send_sems, recv_sems = refs[2 * ns:]
        x, y, c = _position()
        copies = [pltpu.make_async_remote_copy(
            src_ref=_block_of(g_refs[s], 2 * k + 1 - c, rows), dst_ref=land[s].at[k],
            send_sem=send_sems.at[k * ns + s], recv_sem=recv_sems.at[k * ns + s],
            device_id=(x, y, 1 - c), device_id_type=MESH_ID)
            for k in range(N_CHIPS) for s, (_, rows, _) in enumerate(sections)]
        for cp in copies:
            cp.start()
        for cp in copies:
            cp.wait_recv()
        for cp in copies:
            cp.wait_send()

    n = N_CHIPS * ns
    return pl.pallas_call(
        body, name=name,
        out_shape=tuple(jax.ShapeDtypeStruct((N_CHIPS, rows, cols), BF16) for _, rows, cols in sections),
        in_specs=[ANY] * ns, out_specs=(ANY,) * ns,
        scratch_shapes=[pltpu.SemaphoreType.DMA((n,)), pltpu.SemaphoreType.DMA((n,))],
    )(*grads)


def _pair_add(sections, grads, got, core, name):
    ns = len(sections)

    def body(core_ref, *refs):
        g_refs, got_refs, p_refs = refs[:ns], refs[ns:2 * ns], refs[2 * ns:]
        for s in range(ns):
            p_refs[s][0] = (g_refs[s][...].astype(F32) + got_refs[s][0].astype(F32)).astype(BF16)

    slot = [pl.BlockSpec((1, rows, cols), lambda k, c: (k, 0, 0)) for _, rows, cols in sections]
    return pl.pallas_call(
        body, name=name,
        out_shape=tuple(jax.ShapeDtypeStruct((N_CHIPS, rows, cols), BF16) for _, rows, cols in sections),
        grid_spec=pltpu.PrefetchScalarGridSpec(
            num_scalar_prefetch=1, grid=(N_CHIPS,),
            in_specs=[pl.BlockSpec((rows, cols), lambda k, c: (2 * k + c[0], 0)) for _, rows, cols in sections]
            + slot,
            out_specs=tuple(slot)),
        compiler_params=_params(dimension_semantics=("parallel",)),
    )(core, *grads, *got)


def _chip_copies(sections, p_refs, land, send_sems, recv_sems):
    ns = len(sections)
    x, y, c = _position()
    return [pltpu.make_async_remote_copy(
        src_ref=p_refs[s].at[2 * cx + cy], dst_ref=land[s].at[j],
        send_sem=send_sems.at[j * ns + s], recv_sem=recv_sems.at[j * ns + s],
        device_id=(cx, cy, c), device_id_type=MESH_ID)
        for j, (cx, cy) in enumerate(_other_chips(x, y)) for s in range(ns)]


def _chip_exchange(sections, parts, name):
    ns = len(sections)

    def body(*refs):
        copies = _chip_copies(sections, refs[:ns], refs[ns:2 * ns], *refs[2 * ns:])
        for cp in copies:
            cp.start()
        for cp in copies:
            cp.wait_recv()
        for cp in copies:
            cp.wait_send()

    n = 3 * ns
    return pl.pallas_call(
        body, name=name,
        out_shape=tuple(jax.ShapeDtypeStruct((3, rows, cols), BF16) for _, rows, cols in sections),
        in_specs=[ANY] * ns, out_specs=(ANY,) * ns,
        scratch_shapes=[pltpu.SemaphoreType.DMA((n,)), pltpu.SemaphoreType.DMA((n,))],
    )(*parts)


def _chip_exchange_start(sections, parts, name):
    ns = len(sections)

    def body(*refs):
        p_refs, land = refs[:ns], refs[ns:2 * ns]
        send_sems, recv_sems = refs[2 * ns], refs[2 * ns + 1]
        token = refs[-1]
        for cp in _chip_copies(sections, p_refs, land, send_sems, recv_sems):
            cp.start()
        token[...] = jnp.zeros_like(token)

    zones = [lax.empty((3, rows, cols), BF16) for _, rows, cols in sections]
    out = pl.pallas_call(
        body, name=name,
        out_shape=(pltpu.SemaphoreType.DMA((3 * ns,)), pltpu.SemaphoreType.DMA((3 * ns,)),
                   *[pltpu.HBM(a.shape, a.dtype) for a in parts], *[pltpu.HBM(a.shape, a.dtype) for a in zones],
                   jax.ShapeDtypeStruct((8, LANES), F32)),
        in_specs=[HBM] * (2 * ns),
        out_specs=(SEM, SEM, *[HBM] * (2 * ns), pl.BlockSpec(memory_space=pltpu.VMEM)),
        input_output_aliases={i: 2 + i for i in range(2 * ns)},
        compiler_params=pltpu.CompilerParams(has_side_effects=EFFECT),
    )(*[_in_hbm(a) for a in parts], *[_in_hbm(a) for a in zones])
    return out[0], out[1], out[2:2 + ns], out[2 + ns:2 + 2 * ns], out[-1]


def _chip_exchange_wait(sections, send_sems, recv_sems, parts, zones, after, name):
    ns = len(sections)

    def body(*refs):
        p_refs, land = refs[:ns], refs[ns:2 * ns]
        for cp in _chip_copies(sections, p_refs, land, refs[2 * ns], refs[2 * ns + 1]):
            cp.wait_send()
            cp.wait_recv()

    out = pl.pallas_call(
        body, name=name,
        out_shape=tuple(pltpu.HBM(a.shape, a.dtype) for a in (*parts, *zones)),
        in_specs=[HBM] * (2 * ns) + [SEM, SEM, ANY],
        out_specs=(HBM,) * (2 * ns),
        input_output_aliases={i: i for i in range(2 * ns)},
        compiler_params=pltpu.CompilerParams(has_side_effects=EFFECT),
    )(*parts, *zones, send_sems, recv_sems, after)
    return out[:ns], out[ns:]


def _grad_finish(sections, parts, far, chip, name):
    ns = len(sections)

    def body(chip_ref, *refs):
        p_refs, b_refs, g_refs = refs[:ns], refs[ns:2 * ns], refs[2 * ns:]
        for s in range(ns):
            g = p_refs[s][0].astype(F32)
            for j in range(3):
                g = g + b_refs[s][j].astype(F32)
            g_refs[s][...] = g

    half = [(rows // 2, cols) for _, rows, cols in sections]
    return pl.pallas_call(
        body, name=name,
        out_shape=tuple(jax.ShapeDtypeStruct((rows, cols), F32) for _, rows, cols in sections),
        grid_spec=pltpu.PrefetchScalarGridSpec(
            num_scalar_prefetch=1, grid=(2,),
            in_specs=[pl.BlockSpec((1, r, c), lambda i, chip: (chip[0], i, 0)) for r, c in half]
            + [pl.BlockSpec((3, r, c), lambda i, chip: (0, i, 0)) for r, c in half],
            out_specs=tuple(pl.BlockSpec((r, c), lambda i, chip: (i, 0)) for r, c in half)),
        compiler_params=_params(dimension_semantics=("parallel",)),
    )(chip, *parts, *far)


def _sum_devices(parts, rows, name):
    cols = parts.shape[1]
    tr = rows // 2

    def body(*refs):
        s = refs[0][...].astype(F32)
        for d in range(1, N_DEV):
            s = s + refs[d][...].astype(F32)
        refs[N_DEV][...] = s

    return pl.pallas_call(
        body, name=name,
        out_shape=jax.ShapeDtypeStruct((rows, cols), F32),
        grid=(2,),
        in_specs=[pl.BlockSpec((tr, cols), lambda i, d=d: (2 * d + i, 0)) for d in range(N_DEV)],
        out_specs=pl.BlockSpec((tr, cols), lambda i: (i, 0)),
        compiler_params=_params(dimension_semantics=("parallel",)),
    )(*([parts] * N_DEV))


def _adamw(w, g, m, v, name):
    rows, cols = w.shape
    tr = rows
    while tr * cols * 4 > (1 << 20) and tr % 16 == 0:
        tr //= 2
    c1 = 1.0 / (1.0 - ADAM_B1 ** ADAM_STEP)
    c2 = 1.0 / (1.0 - ADAM_B2 ** ADAM_STEP)

    def body(w_ref, g_ref, m_ref, v_ref, d_ref, nm_ref, nv_ref):
        gv = g_ref[...]
        nm = ADAM_B1 * m_ref[...] + (1.0 - ADAM_B1) * gv
        nv = ADAM_B2 * v_ref[...] + (1.0 - ADAM_B2) * (gv * gv)
        nm_ref[...] = nm
        nv_ref[...] = nv
        d_ref[...] = (-ADAM_LR) * ((nm * c1) / (jnp.sqrt(nv * c2) + ADAM_EPS) + ADAM_WD * w_ref[...])

    spec = pl.BlockSpec((tr, cols), lambda i: (i, 0))
    shape = jax.ShapeDtypeStruct((rows, cols), F32)
    return pl.pallas_call(
        body, name=name,
        out_shape=(shape, shape, shape),
        grid=(rows // tr,),
        in_specs=[spec] * 4, out_specs=(spec,) * 3,
        compiler_params=_params(dimension_semantics=("parallel",)),
    )(w, g, m, v)


NAMES = ("ln1_g", "w_in", "b_in", "rpb", "w_att_o", "conv_w", "conv_b", "w_rg_a", "b_rg_a", "w_rg_i",
         "b_rg_i", "lru_lambda", "w_rec_o", "w_out", "ln2_g", "w_ff1", "w_ff2", "lnf_g")
TRANSPOSED = {"w_in": "w_in_t", "w_att_o": "w_att_o_t", "w_ff1": "w_ff1_t"}
ROW_SHARDED = ("w_rec_o", "w_out", "w_ff2")
REPLICATED = (("ln1_g", (1, D)), ("b_in", (1, D_IN)), ("rpb", (N_HEADS * N_RPB_R, N_RPB_C)),
              ("conv_b", (1, D_REC)), ("w_rg_a", (2 * N_REC_BLOCKS * REC_BLOCK, REC_BLOCK)),
              ("w_rg_i", (2 * N_REC_BLOCKS * REC_BLOCK, REC_BLOCK)), ("ln2_g", (1, D)), ("lnf_g", (1, D)))
GATE_BLOCKS = ("w_rg_a", "w_rg_i")
SMALL_ROWS = 112


def _chan_bits(vectors):
    chan = jnp.concatenate(vectors, axis=0)
    bits = lax.bitcast_convert_type(chan, BF16).reshape(-1)
    return jnp.pad(bits, (0, CHAN_BLOCK_ROWS * D - bits.shape[0])).reshape(CHAN_BLOCK_ROWS, D)


def _chan_from_bits(gathered):
    bits = gathered.reshape(N_DEV, CHAN_BLOCK_ROWS * D)[:, :2 * N_CHAN_ROWS * LANES]
    chan = lax.bitcast_convert_type(bits.reshape(N_DEV, N_CHAN_ROWS, LANES, 2), F32)
    return chan.transpose(1, 0, 2).reshape(N_CHAN_ROWS, D)


def kernel(x, ln1_g, w_in, b_in, rpb, w_att_o, conv_w, conv_b, w_rg_a, b_rg_a, w_rg_i, b_rg_i, lru_lambda, w_rec_o, w_out, ln2_g, w_ff1, w_ff2, lnf_g, loss_target, m_ln1_g, m_w_in, m_b_in, m_rpb, m_w_att_o, m_conv_w, m_conv_b, m_w_rg_a, m_b_rg_a, m_w_rg_i, m_b_rg_i, m_lru_lambda, m_w_rec_o, m_w_out, m_ln2_g, m_w_ff1, m_w_ff2, m_lnf_g, v_ln1_g, v_w_in, v_b_in, v_rpb, v_w_att_o, v_conv_w, v_conv_b, v_w_rg_a, v_b_rg_a, v_w_rg_i, v_b_rg_i, v_lru_lambda, v_w_rec_o, v_w_out, v_ln2_g, v_w_ff1, v_w_ff2, v_lnf_g):
    w = dict(zip(NAMES, (ln1_g, w_in, b_in, rpb, w_att_o, conv_w, conv_b, w_rg_a, b_rg_a, w_rg_i,
                         b_rg_i, lru_lambda, w_rec_o, w_out, ln2_g, w_ff1, w_ff2, lnf_g)))
    m = dict(zip(NAMES, (m_ln1_g, m_w_in, m_b_in, m_rpb, m_w_att_o, m_conv_w, m_conv_b, m_w_rg_a,
                         m_b_rg_a, m_w_rg_i, m_b_rg_i, m_lru_lambda, m_w_rec_o, m_w_out, m_ln2_g,
                         m_w_ff1, m_w_ff2, m_lnf_g)))
    v = dict(zip(NAMES, (v_ln1_g, v_w_in, v_b_in, v_rpb, v_w_att_o, v_conv_w, v_conv_b, v_w_rg_a,
                         v_b_rg_a, v_w_rg_i, v_b_rg_i, v_lru_lambda, v_w_rec_o, v_w_out, v_ln2_g,
                         v_w_ff1, v_w_ff2, v_lnf_g)))
    xi, yi, ci = _position()

    shard = {t: w[n][0].T.astype(BF16) for n, t in TRANSPOSED.items()}
    shard.update({n: w[n][0].astype(BF16) for n in ROW_SHARDED})
    shard["chan"] = _chan_bits([w[n][0] for n, _ in CHAN])
    first, later = ("w_in_t", "chan"), ("w_rec_o", "w_out", "w_ff1_t", "w_ff2", "w_att_o_t")
    *gathered, done = _all_gather([shard[n] for n in first], "weight_all_gather")
    p = dict(zip(first, gathered))
    send_sems, recv_sems, sent, zones, token = _gather_start([shard[n] for n in later], done,
                                                             "weight_gather_start")

    def late_weights(after):
        landed = _gather_wait(send_sems, recv_sems, sent, zones, after, "weight_gather_wait")
        return dict(zip(later, _gather_pass_on([shard[n].shape[0] for n in later], landed,
                                               "weight_gather_pass_on")))

    chan = _chan_from_bits(p.pop("chan"))
    r0 = 0
    for n, rows in CHAN:
        p[n] = chan[r0:r0 + rows]
        r0 += rows
    p.update(ln1_g=w["ln1_g"], b_in=w["b_in"] + token[0, 0], rpb=w["rpb"][0], conv_b=w["conv_b"],
             w_rg_a=w["w_rg_a"][0], w_rg_i=w["w_rg_i"][0], ln2_g=w["ln2_g"],
             lnf_g=w["lnf_g"].reshape(1, D))

    core = jnp.reshape(ci, (1,)).astype(jnp.int32)
    chip = jnp.reshape(2 * xi + yi, (1,)).astype(jnp.int32)
    early_sections, late_sections = SECTIONS[1:], SECTIONS[:1]
    in_flight = {}

    def reduce_early(grads):
        chan_g = jnp.concatenate([grads[n] for n, _ in CHAN], axis=0)
        chan_g = chan_g.reshape(N_CHAN_ROWS, N_DEV, LANES).transpose(1, 0, 2).astype(BF16)
        chan_g = jnp.pad(chan_g.reshape(N_DEV, -1), ((0, 0), (0, CHAN_BLOCK_ROWS * D - N_CHAN_ROWS * LANES)))
        grads["chan"] = chan_g.reshape(N_DEV * CHAN_BLOCK_ROWS, D)
        sect = [grads[n] for n, _, _ in early_sections]
        got = _pair_exchange(early_sections, sect, "grad_pair_exchange_early")
        parts = _pair_add(early_sections, sect, got, core, "grad_pair_add_early")
        in_flight["early"] = _chip_exchange_start(early_sections, parts, "grad_chip_exchange_start")
        return in_flight["early"][-1][0, 0]

    loss_part, grad_x, grads = _local_step(x[0], loss_target[0], p, late_weights, reduce_early)
    sect = [grads[n] for n, _, _ in late_sections]
    got = _pair_exchange(late_sections, sect, "grad_pair_exchange_late")
    late_parts = _pair_add(late_sections, sect, got, core, "grad_pair_add_late")
    in_flight["late"] = _chip_exchange_start(late_sections, late_parts, "grad_chip_exchange_start_late")

    flat = jnp.concatenate([grads[n].reshape(-1) for n, _ in REPLICATED if n not in GATE_BLOCKS]
                           + [loss_part.reshape(-1) + in_flight["late"][-1][0, 0]])
    n_small = flat.shape[0]
    flat = jnp.pad(flat, (0, SMALL_ROWS * LANES - n_small)).reshape(SMALL_ROWS, LANES)
    gates = jnp.concatenate([grads[n].reshape(-1, LANES) for n in GATE_BLOCKS], axis=0).astype(BF16)
    small_parts, gate_parts, _ = _all_gather([flat, gates], "small_grad_all_gather")
    small = _sum_devices(small_parts, SMALL_ROWS, "small_grad_sum").reshape(-1)
    gate_sum = _sum_devices(gate_parts, gates.shape[0], "gate_block_grad_sum")
    loss = small[n_small - 1]

    g, delta, new_m, new_v = {}, {}, {}, {}

    def update(n, g2, shape2):
        d2, m2, v2 = _adamw(w[n].reshape(shape2), g2, m[n].reshape(shape2), v[n].reshape(shape2),
                            "adamw_" + n)
        g[n], delta[n], new_m[n], new_v[n] = (a.reshape(w[n].shape) for a in (g2, d2, m2, v2))

    o = 0
    for n, shape2 in REPLICATED:
        if n in GATE_BLOCKS:
            k, rows = GATE_BLOCKS.index(n), gate_sum.shape[0] // len(GATE_BLOCKS)
            update(n, gate_sum[k * rows:(k + 1) * rows].reshape(shape2), shape2)
        else:
            size = shape2[0] * shape2[1]
            update(n, small[o:o + size].reshape(shape2), shape2)
            o += size

    def finish(group, sections, after, name):
        send_sems, recv_sems, parts, zones, _ = in_flight[group]
        parts, far = _chip_exchange_wait(sections, send_sems, recv_sems, parts, zones, after,
                                         "grad_chip_exchange_wait_" + name)
        return dict(zip((n for n, _, _ in sections),
                        _grad_finish(sections, parts, far, chip, "grad_finish_" + name)))

    summed = finish("early", early_sections, delta["lnf_g"], "early")
    for n in ROW_SHARDED:
        update(n, summed[n], summed[n].shape)
    for n, t in TRANSPOSED.items():
        if t in summed:
            update(n, summed[t].T, summed[t].shape[::-1])
    chan_back = summed["chan"].reshape(-1)[:N_CHAN_ROWS * LANES].reshape(N_CHAN_ROWS, LANES)
    r0 = 0
    for n, rows in CHAN:
        update(n, chan_back[r0:r0 + rows], (rows, LANES))
        r0 += rows
    summed = finish("late", late_sections, delta["lru_lambda"], "late")
    update("w_in", summed["w_in_t"].T, summed["w_in_t"].shape[::-1])

    return (loss, grad_x[None], *[g[n] for n in NAMES], *[delta[n] for n in NAMES],
            *[new_m[n] for n in NAMES], *[new_v[n] for n in NAMES])
```

```python
import math

import numpy as np
import jax
import jax.numpy as jnp
from jax import lax
from jax.experimental import pallas as pl
from jax.experimental.pallas import tpu as pltpu

F32 = jnp.float32
BF16 = jnp.bfloat16

T = 2048
D = 1024
D_ATT = 512
D_REC = 1024
D_FF = 4096
D_IN = 5632
N_HEADS = 8
DH = 64
GRID_W = 64
ROWS = T // GRID_W
WIN_H = 8
WIN_W = 16
KWIN = WIN_H * GRID_W
N_RPB_R = 2 * WIN_H - 1
N_RPB_C = 2 * WIN_W - 1
N_REC_BLOCKS = 16
REC_BLOCK = 64
CG = 128
N_CG = D_REC // CG
LRU_C = 8.0
EPS = 1e-6
N_DEV = 8
N_CHIPS = 4
LANES = 128

ADAM_LR = 0.001
ADAM_B1 = 0.9
ADAM_B2 = 0.999
ADAM_EPS = 1e-08
ADAM_WD = 0.01
ADAM_STEP = 10

MESH_AXES = ("x", "y", "c")
VMEM_LIMIT = 56 * 1024 * 1024

TILE = 512
DZ_ARRAYS = ((0, 3, 1), (3, 4, 2), (7, 4, 2))
N_DZ_TILES = D_IN // TILE


def _params(**kw):
    return pltpu.CompilerParams(vmem_limit_bytes=VMEM_LIMIT, **kw)


def _att_tables():
    rq = np.arange(2 * GRID_W) % GRID_W
    kc = np.arange(KWIN) % GRID_W
    win_start = np.clip(rq - WIN_W // 2, 0, GRID_W - WIN_W)
    valid = (kc[None, :] >= win_start[:, None]) & (kc[None, :] < win_start[:, None] + WIN_W)
    return valid.astype(np.float32), _pair_mask()


def _pair_mask():
    half = np.arange(2 * DH) // DH
    return (half[:, None] == half[None, :]).astype(np.float32)


def _dup_table():
    return np.concatenate([np.eye(REC_BLOCK, dtype=np.float32)] * 2, axis=1)


def _sigmoid(x):
    return 0.5 * jnp.tanh(0.5 * x) + 0.5


def _softplus(x):
    return jnp.maximum(x, 0.0) + jnp.log(1.0 + jnp.exp(-jnp.abs(x)))


def _one_minus_square(log_a, a):
    x = 2.0 * log_a
    series = -x * (1.0 + x * (0.5 + x * (1.0 / 6.0)))
    return jnp.where(x > -0.02, series, 1.0 - a * a)


_GELU_C = math.sqrt(2.0 / math.pi)


def _gelu_and_grad(x):
    x2 = x * x
    inner = _GELU_C * (x + 0.044715 * x * x2)
    t = jnp.tanh(inner)
    g = 0.5 * x * (1.0 + t)
    dg = 0.5 * (1.0 + t) + 0.5 * x * (1.0 - t * t) * _GELU_C * (1.0 + 3.0 * 0.044715 * x2)
    return g, dg


def _dot(a, b):
    return jnp.dot(a, b, preferred_element_type=F32)


def _dot_nt(a, b):
    return lax.dot_general(a, b, (((1,), (1,)), ((), ())), preferred_element_type=F32)


def _dot_tn(a, b):
    return lax.dot_general(a, b, (((0,), (0,)), ((), ())), preferred_element_type=F32)


def _dot_exact(a, b):
    return jnp.dot(a, b, precision=lax.Precision.HIGHEST, preferred_element_type=F32)


def _shift_rows(x, s):
    n = x.shape[0]
    rows = lax.broadcasted_iota(jnp.int32, x.shape, 0)
    y = pltpu.roll(x, s % n, 0)
    if s > 0:
        return jnp.where(rows >= s, y, 0.0)
    return jnp.where(rows < n + s, y, 0.0)


def _rms_bwd(dh, xh, r, g):
    dxh = dh * g
    return r * (dxh - xh * jnp.mean(dxh * xh, axis=-1, keepdims=True))


def _matmul(a, b, mode, out_dtype, name, tm=512, tn=1024, tk=2048):
    if mode == "nn":
        (m, k), (k2, n) = a.shape, b.shape
    elif mode == "nt":
        (m, k), (n, k2) = a.shape, b.shape
    else:
        (k, m), (k2, n) = a.shape, b.shape
    assert k == k2
    tm, tn, tk = min(tm, m), min(tn, n), min(tk, k)
    assert m % tm == 0 and n % tn == 0 and k % tk == 0
    nk = k // tk
    dot = {"nn": _dot, "nt": _dot_nt, "tn": _dot_tn}[mode]

    def body(a_ref, b_ref, o_ref, acc):
        kk = pl.program_id(2)
        part = dot(a_ref[...].astype(BF16), b_ref[...].astype(BF16))
        if nk == 1:
            o_ref[...] = part.astype(out_dtype)
            return

        @pl.when(kk == 0)
        def _():
            acc[...] = part

        @pl.when(kk > 0)
        def _():
            acc[...] += part

        @pl.when(kk == nk - 1)
        def _():
            o_ref[...] = acc[...].astype(out_dtype)

    if mode == "tn":
        a_spec = pl.BlockSpec((tk, tm), lambda i, j, kk: (kk, i))
    else:
        a_spec = pl.BlockSpec((tm, tk), lambda i, j, kk: (i, kk))
    if mode == "nt":
        b_spec = pl.BlockSpec((tn, tk), lambda i, j, kk: (j, kk))
    else:
        b_spec = pl.BlockSpec((tk, tn), lambda i, j, kk: (kk, j))
    return pl.pallas_call(
        body, name=name,
        out_shape=jax.ShapeDtypeStruct((m, n), out_dtype),
        grid=(m // tm, n // tn, nk),
        in_specs=[a_spec, b_spec],
        out_specs=pl.BlockSpec((tm, tn), lambda i, j, kk: (i, j)),
        scratch_shapes=[pltpu.VMEM((tm, tn) if nk > 1 else (8, LANES), F32)],
        compiler_params=_params(dimension_semantics=("parallel", "parallel", "arbitrary")),
    )(a, b)


def _in_proj(x, g1, w_in_t, b_in):
    tm = 1024

    def body(x_ref, g_ref, w_ref, b_ref, qkv_ref, uy_ref, gg_ref, h_ref, h_scr):
        j = pl.program_id(1)

        @pl.when(j == 0)
        def _():
            xv = x_ref[...]
            r = lax.rsqrt(jnp.mean(xv * xv, axis=-1, keepdims=True) + EPS)
            h = ((xv * r) * g_ref[...]).astype(BF16)
            h_scr[...] = h
            h_ref[...] = h

        z = _dot_nt(h_scr[...], w_ref[...]) + b_ref[...]

        @pl.when(j < 3)
        def _():
            qkv_ref[...] = z.astype(BF16)

        @pl.when((j >= 3) & (j < 7))
        def _():
            uy_ref[...] = z

        @pl.when(j >= 7)
        def _():
            gg_ref[...] = z

    return pl.pallas_call(
        body, name="in_proj",
        out_shape=(jax.ShapeDtypeStruct((T, 3 * D_ATT), BF16),
                   jax.ShapeDtypeStruct((T, 2 * D_REC), F32),
                   jax.ShapeDtypeStruct((T, 2 * D), F32),
                   jax.ShapeDtypeStruct((T, D), BF16)),
        grid=(T // tm, N_DZ_TILES),
        in_specs=[pl.BlockSpec((tm, D), lambda i, j: (i, 0)),
                  pl.BlockSpec((1, D), lambda i, j: (0, 0)),
                  pl.BlockSpec((TILE, D), lambda i, j: (j, 0)),
                  pl.BlockSpec((1, TILE), lambda i, j: (0, j))],
        out_specs=(pl.BlockSpec((tm, TILE), lambda i, j: (i, jnp.minimum(j, 2))),
                   pl.BlockSpec((tm, TILE), lambda i, j: (i, jnp.clip(j - 3, 0, 3))),
                   pl.BlockSpec((tm, TILE), lambda i, j: (i, jnp.clip(j - 7, 0, 3))),
                   pl.BlockSpec((tm, D), lambda i, j: (i, 0))),
        scratch_shapes=[pltpu.VMEM((tm, D), BF16)],
        compiler_params=_params(dimension_semantics=("parallel", "arbitrary")),
    )(x, g1, w_in_t, b_in)


def _dz_specs(rows, tile_of, row_of):
    def spec(off, n, per_plane):
        def index(*ids):
            t = jnp.clip(tile_of(*ids) - off, 0, n - 1)
            return (t // per_plane, row_of(*ids), t % per_plane)
        return pl.BlockSpec((1, rows, TILE), index)
    return [spec(off, n, per) for off, n, per in DZ_ARRAYS]


def _dh_norm1_bwd(dz, w_in_t, x, g1, dx1):
    tm = 1024

    def body(*refs):
        seg_refs = refs[:3]
        w_ref, x_ref, g_ref, dx1_ref, gx_ref, dg_ref, acc = refs[3:]
        i, kk = pl.program_id(0), pl.program_id(1)

        @pl.when(kk == 0)
        def _():
            acc[...] = jnp.zeros_like(acc)

        for s, (off, n, _) in enumerate(DZ_ARRAYS):
            @pl.when((kk >= off) & (kk < off + n))
            def _(s=s):
                acc[...] += _dot(seg_refs[s][0], w_ref[...])

        @pl.when((i == 0) & (kk == 0))
        def _():
            dg_ref[...] = jnp.zeros_like(dg_ref)

        @pl.when(kk == N_DZ_TILES - 1)
        def _():
            xv = x_ref[...]
            r = lax.rsqrt(jnp.mean(xv * xv, axis=-1, keepdims=True) + EPS)
            xh = xv * r
            dh = acc[...]
            dg_ref[...] += jnp.sum(dh * xh, axis=0, keepdims=True)
            gx_ref[...] = dx1_ref[...] + _rms_bwd(dh, xh, r, g_ref[...])

    tok = pl.BlockSpec((tm, D), lambda i, j: (i, 0))
    vec = pl.BlockSpec((1, D), lambda i, j: (0, 0))
    return pl.pallas_call(
        body, name="dh_norm1_bwd",
        out_shape=(jax.ShapeDtypeStruct((T, D), F32), jax.ShapeDtypeStruct((1, D), F32)),
        grid=(T // tm, N_DZ_TILES),
        in_specs=_dz_specs(tm, lambda i, j: j, lambda i, j: i)
        + [pl.BlockSpec((TILE, D), lambda i, j: (j, 0)), tok, vec, tok],
        out_specs=(tok, vec),
        scratch_shapes=[pltpu.VMEM((tm, D), F32)],
        compiler_params=_params(dimension_semantics=("arbitrary", "arbitrary")),
    )(*dz, w_in_t, x, g1, dx1)


def _grad_w_in(dz, h):
    def body(*refs):
        seg_refs = refs[:3]
        h_ref, gw_ref, gb_ref = refs[3:]
        j = pl.program_id(0)

        for s, (off, n, _) in enumerate(DZ_ARRAYS):
            @pl.when((j >= off) & (j < off + n))
            def _(s=s):
                a = seg_refs[s][0]
                gw_ref[...] = _dot_tn(a, h_ref[...]).astype(BF16)
                gb_ref[...] = jnp.sum(a.astype(F32), axis=0, keepdims=True)

    return pl.pallas_call(
        body, name="grad_w_in",
        out_shape=(jax.ShapeDtypeStruct((D_IN, D), BF16), jax.ShapeDtypeStruct((1, D_IN), F32)),
        grid=(N_DZ_TILES,),
        in_specs=_dz_specs(T, lambda j: j, lambda j: 0) + [pl.BlockSpec((T, D), lambda j: (0, 0))],
        out_specs=(pl.BlockSpec((TILE, D), lambda j: (j, 0)), pl.BlockSpec((1, TILE), lambda j: (0, j))),
        compiler_params=_params(dimension_semantics=("parallel",)),
    )(*dz, h)


def _rpb_rows(rpb):
    padded = jnp.pad(rpb, ((0, 0), (0, 0), (0, GRID_W - N_RPB_C)))
    rows = [padded[:, WIN_H - 1 - oi: 2 * WIN_H - 1 - oi].reshape(N_HEADS // 2, 2, KWIN)
            for oi in range(WIN_H)]
    return jnp.stack(rows, axis=0)


SKEW = KWIN - (WIN_W - 1)


MASKED = -1e30


def _bias_tiles(rows_ref, valid, bias_s):
    for oi in range(WIN_H):
        for hh in range(2):
            row = jnp.broadcast_to(rows_ref[oi, 0, hh:hh + 1, :], (GRID_W, KWIN))
            tile = pltpu.roll(row, SKEW, 1, stride=1, stride_axis=0)
            bias_s[oi, hh * GRID_W:(hh + 1) * GRID_W, :] = jnp.where(valid[:GRID_W], tile, MASKED)


def _bias_tile_grads(gb_s, flip, out_ref):
    for oi in range(WIN_H):
        for hh in range(2):
            g = _dot_exact(flip, gb_s[oi, hh * GRID_W:(hh + 1) * GRID_W, :])
            back = pltpu.roll(g, KWIN - (GRID_W - WIN_W), 1, stride=1, stride_axis=0)
            out_ref[0, oi, hh:hh + 1, :] = jnp.sum(back, axis=0, keepdims=True)


def _rpb_fold(row_grads):
    g = row_grads.transpose(1, 0, 2, 3).reshape(WIN_H, N_HEADS, WIN_H, GRID_W)
    g = g.transpose(0, 2, 1, 3)

    def body(g_ref, o_ref):
        for dr in range(N_RPB_R):
            terms = [g_ref[oi, i] for oi in range(WIN_H) for i in range(WIN_H) if i - oi + WIN_H - 1 == dr]
            acc = terms[0]
            for term in terms[1:]:
                acc = acc + term
            o_ref[dr] = acc

    out = pl.pallas_call(
        body, name="rpb_fold",
        out_shape=jax.ShapeDtypeStruct((N_RPB_R, N_HEADS, GRID_W), F32),
    )(g)
    return out.transpose(1, 0, 2)[:, :, :N_RPB_C]


def _att_scores(q_ref, k_ref, bias_ref, hmask, r):
    rs = jnp.clip(r - WIN_H // 2, 0, ROWS - WIN_H)
    oi = r - rs
    q0 = pl.multiple_of(r * GRID_W, GRID_W)
    k0 = pl.multiple_of(rs * GRID_W, GRID_W)
    q_r = q_ref[pl.ds(q0, GRID_W), :] * (DH ** -0.5)
    q2 = jnp.where(hmask, jnp.concatenate([q_r, q_r], axis=0), jnp.zeros((), BF16))
    kw = k_ref[pl.ds(k0, KWIN), :]
    s = _dot_nt(q2, kw) + bias_ref[oi]
    e = jnp.exp(s - jnp.max(s, axis=-1, keepdims=True))
    return e, 1.0 / jnp.sum(e, axis=-1, keepdims=True), q2, kw, q0, k0, oi


def _att_fwd(qkv, bias_rows):
    valid_np, hmask_np = _att_tables()

    def body(q_ref, k_ref, v_ref, rows_ref, valid_ref, hmask_ref, o_ref, bias_s):
        valid = valid_ref[...] > 0.5
        hmask = hmask_ref[...] > 0.5
        first_head = lax.broadcasted_iota(jnp.int32, (GRID_W, 2 * DH), 1) < DH
        _bias_tiles(rows_ref, valid, bias_s)

        def row(r, carry):
            e, rl, _, _, q0, k0, _ = _att_scores(q_ref, k_ref, bias_s, hmask, r)
            o2 = _dot((e * rl).astype(BF16), v_ref[pl.ds(k0, KWIN), :])
            o_ref[pl.ds(q0, GRID_W), :] = jnp.where(first_head, o2[:GRID_W], o2[GRID_W:]).astype(BF16)
            return carry

        lax.fori_loop(0, ROWS, row, 0, unroll=4)

    col = lambda off: pl.BlockSpec((T, 2 * DH), lambda hp: (0, hp + off))
    return pl.pallas_call(
        body, name="att_fwd",
        out_shape=jax.ShapeDtypeStruct((T, D_ATT), BF16),
        grid=(N_HEADS // 2,),
        in_specs=[col(0), col(4), col(8),
                  pl.BlockSpec((WIN_H, 1, 2, KWIN), lambda hp: (0, hp, 0, 0)),
                  pl.BlockSpec((2 * GRID_W, KWIN), lambda hp: (0, 0)),
                  pl.BlockSpec((2 * DH, 2 * DH), lambda hp: (0, 0))],
        out_specs=pl.BlockSpec((T, 2 * DH), lambda hp: (0, hp)),
        scratch_shapes=[pltpu.VMEM((WIN_H, 2 * GRID_W, KWIN), F32)],
        compiler_params=_params(dimension_semantics=("parallel",)),
    )(qkv, qkv, qkv, bias_rows, jnp.asarray(valid_np), jnp.asarray(hmask_np))


def _att_bwd(qkv, bias_rows, datt, after):
    valid_np, hmask_np = _att_tables()

    def body(q_ref, k_ref, v_ref, do_ref, rows_ref, valid_ref, hmask_ref, flip_ref,
             dqkv_ref, grows_ref, dk_acc, dv_acc, bias_s, gb_s):
        valid = valid_ref[...] > 0.5
        hmask = hmask_ref[...] > 0.5
        first_head = lax.broadcasted_iota(jnp.int32, (GRID_W, 2 * DH), 1) < DH
        dk_acc[...] = jnp.zeros_like(dk_acc)
        dv_acc[...] = jnp.zeros_like(dv_acc)
        gb_s[...] = jnp.zeros_like(gb_s)
        _bias_tiles(rows_ref, valid, bias_s)

        def row(r, carry):
            e, rl, q2, kw, q0, k0, oi = _att_scores(q_ref, k_ref, bias_s, hmask, r)
            do_r = do_ref[pl.ds(q0, GRID_W), :]
            do2 = jnp.where(hmask, jnp.concatenate([do_r, do_r], axis=0), jnp.zeros((), BF16))
            vw = v_ref[pl.ds(k0, KWIN), :]
            p = e * rl
            dp = _dot_nt(do2, vw)
            ds = p * (dp - jnp.sum(dp * p, axis=-1, keepdims=True))
            p16 = p.astype(BF16)
            ds16 = ds.astype(BF16)
            dv_acc[pl.ds(k0, KWIN), :] += _dot_tn(p16, do2)
            dk_acc[pl.ds(k0, KWIN), :] += _dot_tn(ds16, q2)
            dq2 = _dot(ds16, kw) * (DH ** -0.5)
            dqkv_ref[0, pl.ds(q0, GRID_W), :] = jnp.where(first_head, dq2[:GRID_W], dq2[GRID_W:]).astype(BF16)
            gb_s[oi] += ds
            return carry

        lax.fori_loop(0, ROWS, row, 0, unroll=4)
        dqkv_ref[1] = dk_acc[...].astype(BF16)
        dqkv_ref[2] = dv_acc[...].astype(BF16)
        _bias_tile_grads(gb_s, flip_ref[...], grows_ref)

    col = lambda off: pl.BlockSpec((T, 2 * DH), lambda hp: (0, hp + off))
    tiles = pltpu.VMEM((WIN_H, 2 * GRID_W, KWIN), F32)
    return pl.pallas_call(
        body, name="att_bwd",
        out_shape=(jax.ShapeDtypeStruct((3, T, D_ATT), BF16),
                   jax.ShapeDtypeStruct((N_HEADS // 2, WIN_H, 2, KWIN), F32)),
        grid=(N_HEADS // 2,),
        in_specs=[col(0), col(4), col(8), col(0),
                  pl.BlockSpec((WIN_H, 1, 2, KWIN), lambda hp: (0, hp, 0, 0)),
                  pl.BlockSpec((2 * GRID_W, KWIN), lambda hp: (0, 0)),
                  pl.BlockSpec((2 * DH, 2 * DH), lambda hp: (0, 0)),
                  pl.BlockSpec((GRID_W, GRID_W), lambda hp: (0, 0))],
        out_specs=(pl.BlockSpec((3, T, 2 * DH), lambda hp: (0, 0, hp)),
                   pl.BlockSpec((1, WIN_H, 2, KWIN), lambda hp: (hp, 0, 0, 0))),
        scratch_shapes=[pltpu.VMEM((T, 2 * DH), F32), pltpu.VMEM((T, 2 * DH), F32), tiles, tiles],
        compiler_params=_params(dimension_semantics=("parallel",)),
    )(qkv, qkv, qkv, datt, bias_rows, jnp.asarray(valid_np) + after, jnp.asarray(hmask_np),
      jnp.asarray(np.eye(GRID_W, dtype=np.float32)[::-1].copy()))


def _conv_taps(up):
    return (_shift_rows(up, 2), _shift_rows(up, 1), up, _shift_rows(up, -1))


def _pair_block_diag(w_pair, dup, same_half):
    return jnp.where(same_half, _dot(w_pair.astype(BF16), dup), 0.0).astype(BF16)


def _gates(u, u16, wa, ba, wi, bi, lam):
    r = _sigmoid(_dot(u16, wa) + ba)
    ig = _sigmoid(_dot(u16, wi) + bi)
    sp = _softplus(-lam)
    log_a = (-LRU_C) * r * sp
    a = jnp.exp(log_a)
    mult2 = jnp.maximum(_one_minus_square(log_a, a), 0.0)
    return r, ig, sp, a, jnp.sqrt(mult2), mult2


SCAN_BLOCKS = 2


def _scans(jobs):
    c = jobs[0][0].shape[1]
    nblk = T // 8
    rows = lax.broadcasted_iota(jnp.int32, (8, c), 0)

    def block(a, b, reverse):
        for s in (1, 2, 4):
            if reverse:
                keep = rows < 8 - s
                a_s = jnp.where(keep, pltpu.roll(a, 8 - s, 0), 1.0)
                b_s = jnp.where(keep, pltpu.roll(b, 8 - s, 0), 0.0)
            else:
                keep = rows >= s
                a_s = jnp.where(keep, pltpu.roll(a, s, 0), 1.0)
                b_s = jnp.where(keep, pltpu.roll(b, s, 0), 0.0)
            b = a * b_s + b
            a = a * a_s
        return a, b

    def step(i, carry):
        out = []
        for (a_ref, b_ref, h_ref, reverse), h_prev in zip(jobs, carry):
            for u in range(SCAN_BLOCKS):
                blk = i * SCAN_BLOCKS + u
                if reverse:
                    blk = nblk - 1 - blk
                t0 = pl.multiple_of(blk * 8, 8)
                a, b = block(a_ref[pl.ds(t0, 8), :], b_ref[pl.ds(t0, 8), :], reverse)
                h = a * h_prev + b
                h_ref[pl.ds(t0, 8), :] = h
                h_prev = jnp.broadcast_to(h[0:1] if reverse else h[7:8], (8, c))
            out.append(h_prev)
        return tuple(out)

    lax.fori_loop(0, nblk // SCAN_BLOCKS, step, tuple(jnp.zeros((8, c), F32) for _ in jobs))


def _rec_specs():
    tok = lambda off: pl.BlockSpec((T, CG), lambda g: (0, g + off))
    per_ch = lambda rows: pl.BlockSpec((rows, CG), lambda g: (0, g))
    wspec = pl.BlockSpec((2, 1, CG, REC_BLOCK), lambda g: (0, g, 0, 0))
    const = lambda shape: pl.BlockSpec(shape, lambda g: (0, 0))
    return tok, per_ch, wspec, const


def _rec_fwd(uy, conv_w, conv_b, w_a, b_a, w_i, b_i, lam):
    tok, per_ch, wspec, const = _rec_specs()

    def body(up_ref, yb_ref, cw_ref, cb_ref, wa_ref, ba_ref, wi_ref, bi_ref, lam_ref, dup_ref, half_ref,
             hf_ref, hb_ref, yrec_ref, a_f, bx_f, a_b, bx_b):
        dup = dup_ref[...]
        same_half = half_ref[...] > 0.5
        taps = _conv_taps(up_ref[...])
        u = cb_ref[...]
        for j in range(4):
            u = u + taps[j] * cw_ref[j:j + 1, :]
        u16 = u.astype(BF16)
        for d, (a_s, bx_s) in enumerate(((a_f, bx_f), (a_b, bx_b))):
            wa = _pair_block_diag(wa_ref[d, 0], dup, same_half)
            wi = _pair_block_diag(wi_ref[d, 0], dup, same_half)
            _, ig, _, a, mult, _ = _gates(u, u16, wa, ba_ref[d:d + 1, :], wi, bi_ref[d:d + 1, :],
                                       lam_ref[d:d + 1, :])
            a_s[...] = a
            bx_s[...] = mult * (ig * u)
        _scans([(a_f, bx_f, hf_ref, False), (a_b, bx_b, hb_ref, True)])
        gelu, _ = _gelu_and_grad(yb_ref[...])
        yrec_ref[...] = ((hf_ref[...] + hb_ref[...]) * gelu).astype(BF16)

    return pl.pallas_call(
        body, name="rec_fwd",
        out_shape=(jax.ShapeDtypeStruct((T, D_REC), F32), jax.ShapeDtypeStruct((T, D_REC), F32),
                   jax.ShapeDtypeStruct((T, D_REC), BF16)),
        grid=(N_CG,),
        in_specs=[tok(0), tok(N_CG), per_ch(4), per_ch(1), wspec, per_ch(2), wspec, per_ch(2), per_ch(2),
                  const((REC_BLOCK, CG)), const((CG, CG))],
        out_specs=(tok(0), tok(0), tok(0)),
        scratch_shapes=[pltpu.VMEM((T, CG), F32)] * 4,
        compiler_params=_params(dimension_semantics=("parallel",)),
    )(uy, uy, conv_w, conv_b, w_a, b_a, w_i, b_i, lam,
      jnp.asarray(_dup_table(), BF16), jnp.asarray(_pair_mask()))


def _rec_bwd(uy, hf, hb, dyrec, conv_w, conv_b, w_a, b_a, w_i, b_i, lam):
    tok, per_ch, wspec, const = _rec_specs()

    def body(up_ref, yb_ref, hf_ref, hb_ref, dy_ref, cw_ref, cb_ref, wa_ref, ba_ref, wi_ref, bi_ref,
             lam_ref, dup_ref, dupt_ref, half_ref,
             duy_ref, dcw_ref, dcb_ref, dwa_ref, dba_ref, dwi_ref, dbi_ref, dlam_ref,
             a_s0, a_s1, dh_s, g_s0, g_s1):
        dup = dup_ref[...]
        dup_t = dupt_ref[...]
        same_half = half_ref[...] > 0.5
        taps = _conv_taps(up_ref[...])
        u = cb_ref[...]
        for j in range(4):
            u = u + taps[j] * cw_ref[j:j + 1, :]
        u16 = u.astype(BF16)
        gelu, dgelu = _gelu_and_grad(yb_ref[...])
        dy = dy_ref[...]
        duy_ref[1] = (dy * (hf_ref[...] + hb_ref[...]) * dgelu).astype(BF16)
        dh_s[...] = dy * gelu
        gate_values = []
        for d, a_s in enumerate((a_s0, a_s1)):
            wa = _pair_block_diag(wa_ref[d, 0], dup, same_half)
            wi = _pair_block_diag(wi_ref[d, 0], dup, same_half)
            lam_d = lam_ref[d:d + 1, :]
            r, ig, sp, a, mult, mult2 = _gates(u, u16, wa, ba_ref[d:d + 1, :], wi, bi_ref[d:d + 1, :], lam_d)
            a_s[...] = _shift_rows(a, 1 if d == 1 else -1)
            gate_values.append((wa, wi, lam_d, r, ig, sp, a, mult, mult2))
        _scans([(a_s0, dh_s, g_s0, True), (a_s1, dh_s, g_s1, False)])
        du = jnp.zeros((T, CG), F32)
        for d, g_s in enumerate((g_s0, g_s1)):
            reverse = d == 1
            wa, wi, lam_d, r, ig, sp, a, mult, mult2 = gate_values[d]
            g = g_s[...]
            h_prev = _shift_rows(hb_ref[...], -1) if reverse else _shift_rows(hf_ref[...], 1)
            da = g * h_prev
            dmult = g * (ig * u)
            dig = g * mult * u
            du = du + g * mult * ig
            dmult_dlog = jnp.where(mult2 > 0.0, -(a * a) * lax.rsqrt(mult2), 0.0)
            dlog_a = da * a + dmult * dmult_dlog
            dr = dlog_a * ((-LRU_C) * sp)
            dsp = jnp.sum(dlog_a * ((-LRU_C) * r), axis=0, keepdims=True)
            dlam_ref[d:d + 1, :] = dsp * (-_sigmoid(-lam_d))
            dga = dr * r * (1.0 - r)
            dgi = dig * ig * (1.0 - ig)
            dga16 = dga.astype(BF16)
            dgi16 = dgi.astype(BF16)
            du = du + _dot_nt(dga16, wa) + _dot_nt(dgi16, wi)
            dwa_ref[d, 0] = _dot_exact(jnp.where(same_half, _dot_tn(u16, dga16), 0.0), dup_t)
            dwi_ref[d, 0] = _dot_exact(jnp.where(same_half, _dot_tn(u16, dgi16), 0.0), dup_t)
            dba_ref[d:d + 1, :] = jnp.sum(dga, axis=0, keepdims=True)
            dbi_ref[d:d + 1, :] = jnp.sum(dgi, axis=0, keepdims=True)
        dcb_ref[...] = jnp.sum(du, axis=0, keepdims=True)
        for j in range(4):
            dcw_ref[j:j + 1, :] = jnp.sum(du * taps[j], axis=0, keepdims=True)
        dup_in = (_shift_rows(du, -2) * cw_ref[0:1, :] + _shift_rows(du, -1) * cw_ref[1:2, :]
                  + du * cw_ref[2:3, :] + _shift_rows(du, 1) * cw_ref[3:4, :])
        duy_ref[0] = dup_in.astype(BF16)

    wshape = jax.ShapeDtypeStruct((2, N_CG, CG, REC_BLOCK), F32)
    vec = lambda rows: jax.ShapeDtypeStruct((rows, D_REC), F32)
    dup_np = _dup_table()
    return pl.pallas_call(
        body, name="rec_bwd",
        out_shape=(jax.ShapeDtypeStruct((2, T, D_REC), BF16),
                   vec(4), vec(1), wshape, vec(2), wshape, vec(2), vec(2)),
        grid=(N_CG,),
        in_specs=[tok(0), tok(N_CG), tok(0), tok(0), tok(0),
                  per_ch(4), per_ch(1), wspec, per_ch(2), wspec, per_ch(2), per_ch(2),
                  const((REC_BLOCK, CG)), const((CG, REC_BLOCK)), const((CG, CG))],
        out_specs=(pl.BlockSpec((2, T, CG), lambda g: (0, 0, g)),
                   per_ch(4), per_ch(1), wspec, per_ch(2), wspec, per_ch(2), per_ch(2)),
        scratch_shapes=[pltpu.VMEM((T, CG), F32)] * 5,
        compiler_params=_params(dimension_semantics=("parallel",)),
    )(uy, uy, hf, hb, dyrec, conv_w, conv_b, w_a, b_a, w_i, b_i, lam,
      jnp.asarray(dup_np, BF16), jnp.asarray(dup_np.T.copy()), jnp.asarray(_pair_mask()))


TM_MIX = 512


def _mix_specs():
    tok = lambda width, blk=0: pl.BlockSpec((TM_MIX, width), lambda i: (i, blk))
    full = lambda shape: pl.BlockSpec(shape, lambda i: (0, 0))
    return tok, full


def _mix_fwd(x, att, yrec, gg, w_att_o_t, w_rec_o, w_out):
    tok, full = _mix_specs()

    def body(x_ref, att_ref, yr_ref, ga_ref, gr_ref, wao_ref, wro_ref, wo_ref, x1_ref, mixed_ref):
        y_att = _dot_nt(att_ref[...], wao_ref[...])
        y_rec = _dot(yr_ref[...], wro_ref[...])
        mixed = (_sigmoid(ga_ref[...]) * y_att + _sigmoid(gr_ref[...]) * y_rec).astype(BF16)
        mixed_ref[...] = mixed
        x1_ref[...] = x_ref[...] + _dot(mixed, wo_ref[...])

    return pl.pallas_call(
        body, name="mix_fwd",
        out_shape=(jax.ShapeDtypeStruct((T, D), F32), jax.ShapeDtypeStruct((T, D), BF16)),
        grid=(T // TM_MIX,),
        in_specs=[tok(D), tok(D_ATT), tok(D_REC), tok(D, 0), tok(D, 1),
                  full((D, D_ATT)), full((D_REC, D)), full((D, D))],
        out_specs=(tok(D), tok(D)),
        compiler_params=_params(dimension_semantics=("parallel",)),
    )(x, att, yrec, gg, gg, w_att_o_t, w_rec_o, w_out)


def _mix_bwd(dx1, att, yrec, gg, w_att_o_t, w_rec_o, w_out):
    tok, full = _mix_specs()

    def body(dx_ref, att_ref, yr_ref, ga_ref, gr_ref, wao_ref, wro_ref, wo_ref,
             dgg_ref, dya_ref, dyr_ref, datt_ref, dyrp_ref):
        dmixed = _dot_nt(dx_ref[...].astype(BF16), wo_ref[...])
        y_att = _dot_nt(att_ref[...], wao_ref[...])
        y_rec = _dot(yr_ref[...], wro_ref[...])
        sa = _sigmoid(ga_ref[...])
        sr = _sigmoid(gr_ref[...])
        dgg_ref[0] = (dmixed * y_att * sa * (1.0 - sa)).astype(BF16)
        dgg_ref[1] = (dmixed * y_rec * sr * (1.0 - sr)).astype(BF16)
        dya = (dmixed * sa).astype(BF16)
        dyr = (dmixed * sr).astype(BF16)
        dya_ref[...] = dya
        dyr_ref[...] = dyr
        datt_ref[...] = _dot(dya, wao_ref[...]).astype(BF16)
        dyrp_ref[...] = _dot_nt(dyr, wro_ref[...])

    return pl.pallas_call(
        body, name="mix_bwd",
        out_shape=(jax.ShapeDtypeStruct((2, T, D), BF16),
                   jax.ShapeDtypeStruct((T, D), BF16), jax.ShapeDtypeStruct((T, D), BF16),
                   jax.ShapeDtypeStruct((T, D_ATT), BF16), jax.ShapeDtypeStruct((T, D_REC), F32)),
        grid=(T // TM_MIX,),
        in_specs=[tok(D), tok(D_ATT), tok(D_REC), tok(D, 0), tok(D, 1),
                  full((D, D_ATT)), full((D_REC, D)), full((D, D))],
        out_specs=(pl.BlockSpec((2, TM_MIX, D), lambda i: (0, i, 0)),
                   tok(D), tok(D), tok(D_ATT), tok(D_REC)),
        compiler_params=_params(dimension_semantics=("parallel",)),
    )(dx1, att, yrec, gg, gg, w_att_o_t, w_rec_o, w_out)


TM_FFN = 256
FF_CHUNK = 1024


def _ffn_loss(x1, target, g2, gf, w_ff1_t, w_ff2):
    n_chunks = D_FF // FF_CHUNK

    def body(x1_ref, tg_ref, g2_ref, gf_ref, w1_hbm, w2_hbm,
             loss_ref, dx1_ref, h2_ref, act_ref, dpre_ref, dx2_ref, dg2_ref, dgf_ref,
             w1, w2, relu_s):
        i = pl.program_id(0)

        @pl.when(i == 0)
        def _():
            pltpu.sync_copy(w1_hbm, w1)
            pltpu.sync_copy(w2_hbm, w2)
            loss_ref[...] = jnp.zeros_like(loss_ref)
            dg2_ref[...] = jnp.zeros_like(dg2_ref)
            dgf_ref[...] = jnp.zeros_like(dgf_ref)

        x1v = x1_ref[...]
        r2 = lax.rsqrt(jnp.mean(x1v * x1v, axis=-1, keepdims=True) + EPS)
        xh2 = x1v * r2
        h2 = (xh2 * g2_ref[...]).astype(BF16)
        h2_ref[...] = h2
        x2 = x1v
        for c in range(n_chunks):
            ff = slice(c * FF_CHUNK, (c + 1) * FF_CHUNK)
            rl = jnp.maximum(_dot_nt(h2, w1[ff, :]), 0.0)
            relu_s[:, ff] = rl
            act = (rl * rl).astype(BF16)
            act_ref[:, ff] = act
            x2 = x2 + _dot(act, w2[ff, :])
        r3 = lax.rsqrt(jnp.mean(x2 * x2, axis=-1, keepdims=True) + EPS)
        xh3 = x2 * r3
        err = xh3 * gf_ref[...] - tg_ref[...]
        loss_ref[...] += 0.5 * jnp.sum(jnp.mean(err * err, axis=-1, keepdims=True))
        dy = err * (1.0 / D)
        dgf_ref[...] += jnp.sum(dy * xh3, axis=0, keepdims=True)
        dx2 = _rms_bwd(dy, xh3, r3, gf_ref[...])
        dx2_16 = dx2.astype(BF16)
        dx2_ref[...] = dx2_16
        dh2 = jnp.zeros((TM_FFN, D), F32)
        for c in range(n_chunks):
            ff = slice(c * FF_CHUNK, (c + 1) * FF_CHUNK)
            dpre = (_dot_nt(dx2_16, w2[ff, :]) * (2.0 * relu_s[:, ff])).astype(BF16)
            dpre_ref[:, ff] = dpre
            dh2 = dh2 + _dot(dpre, w1[ff, :])
        dg2_ref[...] += jnp.sum(dh2 * xh2, axis=0, keepdims=True)
        dx1_ref[...] = dx2 + _rms_bwd(dh2, xh2, r2, g2_ref[...])

    tok = lambda width: pl.BlockSpec((TM_FFN, width), lambda i: (i, 0))
    vec = pl.BlockSpec((1, D), lambda i: (0, 0))
    hbm = pl.BlockSpec(memory_space=pl.ANY)
    return pl.pallas_call(
        body, name="ffn_loss",
        out_shape=(jax.ShapeDtypeStruct((8, 128), F32), jax.ShapeDtypeStruct((T, D), F32),
                   jax.ShapeDtypeStruct((T, D), BF16), jax.ShapeDtypeStruct((T, D_FF), BF16),
                   jax.ShapeDtypeStruct((T, D_FF), BF16), jax.ShapeDtypeStruct((T, D), BF16),
                   jax.ShapeDtypeStruct((1, D), F32), jax.ShapeDtypeStruct((1, D), F32)),
        grid=(T // TM_FFN,),
        in_specs=[tok(D), tok(D), vec, vec, hbm, hbm],
        out_specs=(pl.BlockSpec((8, 128), lambda i: (0, 0)), tok(D), tok(D), tok(D_FF), tok(D_FF), tok(D),
                   vec, vec),
        scratch_shapes=[pltpu.VMEM((D_FF, D), BF16), pltpu.VMEM((D_FF, D), BF16),
                        pltpu.VMEM((TM_FFN, D_FF), F32)],
        compiler_params=_params(dimension_semantics=("arbitrary",)),
    )(x1, target, g2, gf, w_ff1_t, w_ff2)


def _local_step(x, target, p, late_weights, reduce_early):
    bias = _rpb_rows(p["rpb"])
    pairs = lambda w: w.reshape(2, N_CG, CG, REC_BLOCK)
    w_a, w_i = pairs(p["w_rg_a"]), pairs(p["w_rg_i"])
    rec_params = (p["conv_w"], p["conv_b"], w_a, p["b_rg_a"], w_i, p["b_rg_i"], p["lru_lambda"])

    qkv, uy, gg, h = _in_proj(x, p["ln1_g"], p["w_in_t"], p["b_in"])
    att = _att_fwd(qkv, bias)
    hf, hb, yrec = _rec_fwd(uy, *rec_params)
    p = {**p, **late_weights(yrec)}
    x1, mixed = _mix_fwd(x, att, yrec, gg, p["w_att_o_t"], p["w_rec_o"], p["w_out"])
    loss8, dx1, h2, act, dpre, dx2, g_ln2, g_lnf = _ffn_loss(
        x1, target, p["ln2_g"], p["lnf_g"], p["w_ff1_t"], p["w_ff2"])

    dgg, dya, dyr, datt, dyrp = _mix_bwd(dx1, att, yrec, gg, p["w_att_o_t"], p["w_rec_o"], p["w_out"])
    duy, g_cw, g_cb, g_wa, g_ba, g_wi, g_bi, g_lam = _rec_bwd(uy, hf, hb, dyrp, *rec_params)
    blocks = lambda g: g.reshape(2, N_REC_BLOCKS, REC_BLOCK, REC_BLOCK)
    grads = {
        "w_att_o_t": _matmul(dya, att, "tn", BF16, "g_w_att_o"),
        "conv_w": g_cw, "conv_b": g_cb, "w_rg_a": blocks(g_wa), "b_rg_a": g_ba,
        "w_rg_i": blocks(g_wi), "b_rg_i": g_bi, "lru_lambda": g_lam,
        "w_rec_o": _matmul(yrec, dyr, "tn", BF16, "g_w_rec_o"),
        "w_out": _matmul(mixed, dx1, "tn", BF16, "g_w_out"),
        "ln2_g": g_ln2,
        "w_ff1_t": _matmul(dpre, h2, "tn", BF16, "g_w_ff1"),
        "w_ff2": _matmul(act, dx2, "tn", BF16, "g_w_ff2"),
        "lnf_g": g_lnf,
    }
    after = reduce_early(grads)
    dqkv, gbias = _att_bwd(qkv, bias, datt, after)
    dz = (dqkv, duy, dgg)
    grad_x, g_ln1 = _dh_norm1_bwd(dz, p["w_in_t"], x, p["ln1_g"], dx1)
    g_w_in_t, g_b_in = _grad_w_in(dz, h)
    grads.update(ln1_g=g_ln1, w_in_t=g_w_in_t, b_in=g_b_in, rpb=_rpb_fold(gbias))
    return loss8[0:1, 0:1], grad_x, grads


MESH_ID = pl.DeviceIdType.MESH
ANY = pl.BlockSpec(memory_space=pl.ANY)

CHAN_BLOCK_ROWS = 32
SECTIONS = (("w_in_t", 704, D), ("w_rec_o", 128, D), ("w_out", 128, D), ("w_ff1_t", 512, D),
            ("w_ff2", 512, D), ("chan", CHAN_BLOCK_ROWS, D), ("w_att_o_t", 128, D_ATT))
N_SEC = len(SECTIONS)
N_CHAN_ROWS = 10
CHAN = (("conv_w", 4), ("b_rg_a", 2), ("b_rg_i", 2), ("lru_lambda", 2))


def _position():
    return lax.axis_index("x"), lax.axis_index("y"), lax.axis_index("c")


def _other_chips(x, y):
    return [(1 - x, y), (x, 1 - y), (1 - x, 1 - y)]


def _block_of(ref, dev, rows):
    return ref.at[pl.ds(pl.multiple_of(dev * rows, 16), rows)]


def _all_gather(shards, name):
    ns = len(shards)

    def body(*refs):
        x_refs, out_refs, done_ref = refs[:ns], refs[ns:2 * ns], refs[2 * ns]
        send_sems, recv_sems, local_sems = refs[2 * ns + 1:]
        done_ref[0, 0] = 0.0
        x, y, c = _position()
        me, sibling = (x, y, c), (x, y, 1 - c)
        chips = _other_chips(x, y)

        def rows(s, px, py, pc):
            return _block_of(out_refs[s], 4 * px + 2 * py + pc, shards[s].shape[0])

        def copy(k, s, block, to, from_shard=False):
            return pltpu.make_async_remote_copy(
                src_ref=x_refs[s] if from_shard else rows(s, *block), dst_ref=rows(s, *block),
                send_sem=send_sems.at[k * ns + s], recv_sem=recv_sems.at[k * ns + s],
                device_id=to, device_id_type=MESH_ID)

        sections = range(ns)
        mine = [pltpu.make_async_copy(x_refs[s], rows(s, *me), local_sems.at[s]) for s in sections]
        first = [copy(0, s, me, sibling, True) for s in sections]
        first += [copy(1 + j, s, me, (*chip, c), True) for j, chip in enumerate(chips) for s in sections]
        for cp in mine + first:
            cp.start()
        passed = []
        for j, chip in enumerate(chips):
            for s in sections:
                copy(1 + j, s, (*chip, c), me).wait_recv()
                passed.append(copy(4 + j, s, (*chip, c), sibling))
                passed[-1].start()
        for s in sections:
            copy(0, s, sibling, me).wait_recv()
        for j, chip in enumerate(chips):
            for s in sections:
                copy(4 + j, s, (*chip, 1 - c), me).wait_recv()
        for cp in first + passed:
            cp.wait_send()
        for cp in mine:
            cp.wait()

    return pl.pallas_call(
        body, name=name,
        out_shape=tuple(jax.ShapeDtypeStruct((N_DEV * s.shape[0], s.shape[1]), s.dtype) for s in shards)
        + (jax.ShapeDtypeStruct((1, 1), F32),),
        in_specs=[ANY] * ns,
        out_specs=(ANY,) * ns + (pl.BlockSpec(memory_space=pltpu.SMEM),),
        scratch_shapes=[pltpu.SemaphoreType.DMA((7 * ns,)), pltpu.SemaphoreType.DMA((7 * ns,)),
                        pltpu.SemaphoreType.DMA((ns,))],
    )(*shards)


HBM = pl.BlockSpec(memory_space=pltpu.HBM)
SEM = pl.BlockSpec(memory_space=pltpu.SEMAPHORE)
EFFECT = pltpu.SideEffectType.DATAFLOW_SIDE_EFFECTING


def _in_hbm(a):
    return pltpu.with_memory_space_constraint(a, pltpu.HBM)


def _first_hop_copies(shards, x_refs, zones, send_sems, recv_sems):
    ns = len(shards)
    x, y, c = _position()
    targets = [(x, y, 1 - c)] + [(cx, cy, c) for cx, cy in _other_chips(x, y)]
    return [pltpu.make_async_remote_copy(
        src_ref=x_refs[s], dst_ref=_block_of(zones[s], 4 * x + 2 * y + c, shards[s].shape[0]),
        send_sem=send_sems.at[k * ns + s], recv_sem=recv_sems.at[k * ns + s],
        device_id=to, device_id_type=MESH_ID)
        for k, to in enumerate(targets) for s in range(ns)]


def _own_blocks_placed(shards, after):
    ns = len(shards)
    x, y, c = _position()
    me = jnp.reshape(4 * x + 2 * y + c, (1,)).astype(jnp.int32)
    shards = [*shards[:-1], shards[-1] + after.astype(shards[-1].dtype)]

    def body(me_ref, *refs):
        for s in range(ns):
            refs[ns + s][...] = refs[s][...]

    return pl.pallas_call(
        body, name="own_blocks_placed",
        out_shape=tuple(jax.ShapeDtypeStruct((N_DEV * s.shape[0], s.shape[1]), s.dtype) for s in shards),
        grid_spec=pltpu.PrefetchScalarGridSpec(
            num_scalar_prefetch=1, grid=(1,),
            in_specs=[pl.BlockSpec(s.shape, lambda i, me: (0, 0)) for s in shards],
            out_specs=tuple(pl.BlockSpec(s.shape, lambda i, me: (me[0], 0)) for s in shards)),
        compiler_params=_params(dimension_semantics=("arbitrary",)),
    )(me, *shards)


def _gather_start(shards, after, name):
    ns = len(shards)
    zones = _own_blocks_placed(shards, after)

    def body(*refs):
        for cp in _first_hop_copies(shards, refs[:ns], refs[ns:2 * ns], refs[2 * ns], refs[2 * ns + 1]):
            cp.start()
        refs[-1][...] = jnp.zeros_like(refs[-1])

    out = pl.pallas_call(
        body, name=name,
        out_shape=(pltpu.SemaphoreType.DMA((4 * ns,)), pltpu.SemaphoreType.DMA((4 * ns,)),
                   *[pltpu.HBM(a.shape, a.dtype) for a in (*shards, *zones)],
                   jax.ShapeDtypeStruct((8, LANES), F32)),
        in_specs=[HBM] * (2 * ns),
        out_specs=(SEM, SEM, *[HBM] * (2 * ns), pl.BlockSpec(memory_space=pltpu.VMEM)),
        input_output_aliases={i: 2 + i for i in range(2 * ns)},
        compiler_params=pltpu.CompilerParams(has_side_effects=EFFECT),
    )(*[_in_hbm(a) for a in shards], *[_in_hbm(a) for a in zones])
    return out[0], out[1], out[2:2 + ns], out[2 + ns:2 + 2 * ns], out[-1]


def _gather_wait(send_sems, recv_sems, shards, zones, after, name):
    ns = len(shards)

    def body(*refs):
        for cp in _first_hop_copies(shards, refs[:ns], refs[ns:2 * ns], refs[2 * ns], refs[2 * ns + 1]):
            cp.wait_send()
            cp.wait_recv()

    out = pl.pallas_call(
        body, name=name,
        out_shape=tuple(pltpu.HBM(a.shape, a.dtype) for a in (*shards, *zones)),
        in_specs=[HBM] * (2 * ns) + [SEM, SEM, ANY],
        out_specs=(HBM,) * (2 * ns),
        input_output_aliases={i: i for i in range(2 * ns)},
        compiler_params=pltpu.CompilerParams(has_side_effects=EFFECT),
    )(*shards, *zones, send_sems, recv_sems, after)
    return out[ns:]


def _gather_pass_on(rows, zones, name):
    ns = len(zones)

    def body(*refs):
        in_refs, out_refs = refs[:ns], refs[ns:2 * ns]
        send_sems, recv_sems = refs[2 * ns:]
        x, y, c = _position()
        copies = [pltpu.make_async_remote_copy(
            src_ref=_block_of(in_refs[s], 4 * cx + 2 * cy + c, rows[s]),
            dst_ref=_block_of(out_refs[s], 4 * cx + 2 * cy + c, rows[s]),
            send_sem=send_sems.at[j * ns + s], recv_sem=recv_sems.at[j * ns + s],
            device_id=(x, y, 1 - c), device_id_type=MESH_ID)
            for j, (cx, cy) in enumerate(_other_chips(x, y)) for s in range(ns)]
        for cp in copies:
            cp.start()
        for cp in copies:
            cp.wait_recv()
        for cp in copies:
            cp.wait_send()

    return pl.pallas_call(
        body, name=name,
        out_shape=tuple(jax.ShapeDtypeStruct(z.shape, z.dtype) for z in zones),
        in_specs=[ANY] * ns, out_specs=(ANY,) * ns,
        input_output_aliases={i: i for i in range(ns)},
        scratch_shapes=[pltpu.SemaphoreType.DMA((3 * ns,)), pltpu.SemaphoreType.DMA((3 * ns,))],
    )(*zones)


def _pair_exchange(sections, grads, name):
    ns = len(sections)

    def body(*refs):
        g_refs, land = refs[:ns], refs[ns:2 * ns]
        send_sems, recv_sems = refs[2 * ns:]
        x, y, c = _position()
        copies = [pltpu.make_async_remote_copy(
            src_ref=_block_of(g_refs[s], 2 * k + 1 - c, rows), dst_ref=land[s].at[k],
            send_sem=send_sems.at[k * ns + s], recv_sem=recv_sems.at[k * ns + s],
            device_id=(x, y, 1 - c), device_id_type=MESH_ID)
            for k in range(N_CHIPS) for s, (_, rows, _) in enumerate(sections)]
        for cp in copies:
            cp.start()
        for cp in copies:
            cp.wait_recv()
        for cp in copies:
            cp.wait_send()

    n = N_CHIPS * ns
    return pl.pallas_call(
        body, name=name,
        out_shape=tuple(jax.ShapeDtypeStruct((N_CHIPS, rows, cols), BF16) for _, rows, cols in sections),
        in_specs=[ANY] * ns, out_specs=(ANY,) * ns,
        scratch_shapes=[pltpu.SemaphoreType.DMA((n,)), pltpu.SemaphoreType.DMA((n,))],
    )(*grads)


def _pair_add(sections, grads, got, core, name):
    ns = len(sections)

    def body(core_ref, *refs):
        g_refs, got_refs, p_refs = refs[:ns], refs[ns:2 * ns], refs[2 * ns:]
        for s in range(ns):
            p_refs[s][0] = (g_refs[s][...].astype(F32) + got_refs[s][0].astype(F32)).astype(BF16)

    slot = [pl.BlockSpec((1, rows, cols), lambda k, c: (k, 0, 0)) for _, rows, cols in sections]
    return pl.pallas_call(
        body, name=name,
        out_shape=tuple(jax.ShapeDtypeStruct((N_CHIPS, rows, cols), BF16) for _, rows, cols in sections),
        grid_spec=pltpu.PrefetchScalarGridSpec(
            num_scalar_prefetch=1, grid=(N_CHIPS,),
            in_specs=[pl.BlockSpec((rows, cols), lambda k, c: (2 * k + c[0], 0)) for _, rows, cols in sections]
            + slot,
            out_specs=tuple(slot)),
        compiler_params=_params(dimension_semantics=("parallel",)),
    )(core, *grads, *got)


def _chip_copies(sections, p_refs, land, send_sems, recv_sems):
    ns = len(sections)
    x, y, c = _position()
    return [pltpu.make_async_remote_copy(
        src_ref=p_refs[s].at[2 * cx + cy], dst_ref=land[s].at[j],
        send_sem=send_sems.at[j * ns + s], recv_sem=recv_sems.at[j * ns + s],
        device_id=(cx, cy, c), device_id_type=MESH_ID)
        for j, (cx, cy) in enumerate(_other_chips(x, y)) for s in range(ns)]


def _chip_exchange(sections, parts, name):
    ns = len(sections)

    def body(*refs):
        copies = _chip_copies(sections, refs[:ns], refs[ns:2 * ns], *refs[2 * ns:])
        for cp in copies:
            cp.start()
        for cp in copies:
            cp.wait_recv()
        for cp in copies:
            cp.wait_send()

    n = 3 * ns
    return pl.pallas_call(
        body, name=name,
        out_shape=tuple(jax.ShapeDtypeStruct((3, rows, cols), BF16) for _, rows, cols in sections),
        in_specs=[ANY] * ns, out_specs=(ANY,) * ns,
        scratch_shapes=[pltpu.SemaphoreType.DMA((n,)), pltpu.SemaphoreType.DMA((n,))],
    )(*parts)


def _chip_exchange_start(sections, parts, name):
    ns = len(sections)

    def body(*refs):
        p_refs, land = refs[:ns], refs[ns:2 * ns]
        send_sems, recv_sems = refs[2 * ns], refs[2 * ns + 1]
        token = refs[-1]
        for cp in _chip_copies(sections, p_refs, land, send_sems, recv_sems):
            cp.start()
        token[...] = jnp.zeros_like(token)

    zones = [lax.empty((3, rows, cols), BF16) for _, rows, cols in sections]
    out = pl.pallas_call(
        body, name=name,
        out_shape=(pltpu.SemaphoreType.DMA((3 * ns,)), pltpu.SemaphoreType.DMA((3 * ns,)),
                   *[pltpu.HBM(a.shape, a.dtype) for a in parts], *[pltpu.HBM(a.shape, a.dtype) for a in zones],
                   jax.ShapeDtypeStruct((8, LANES), F32)),
        in_specs=[HBM] * (2 * ns),
        out_specs=(SEM, SEM, *[HBM] * (2 * ns), pl.BlockSpec(memory_space=pltpu.VMEM)),
        input_output_aliases={i: 2 + i for i in range(2 * ns)},
        compiler_params=pltpu.CompilerParams(has_side_effects=EFFECT),
    )(*[_in_hbm(a) for a in parts], *[_in_hbm(a) for a in zones])
    return out[0], out[1], out[2:2 + ns], out[2 + ns:2 + 2 * ns], out[-1]


def _chip_exchange_wait(sections, send_sems, recv_sems, parts, zones, after, name):
    ns = len(sections)

    def body(*refs):
        p_refs, land = refs[:ns], refs[ns:2 * ns]
        for cp in _chip_copies(sections, p_refs, land, refs[2 * ns], refs[2 * ns + 1]):
            cp.wait_send()
            cp.wait_recv()

    out = pl.pallas_call(
        body, name=name,
        out_shape=tuple(pltpu.HBM(a.shape, a.dtype) for a in (*parts, *zones)),
        in_specs=[HBM] * (2 * ns) + [SEM, SEM, ANY],
        out_specs=(HBM,) * (2 * ns),
        input_output_aliases={i: i for i in range(2 * ns)},
        compiler_params=pltpu.CompilerParams(has_side_effects=EFFECT),
    )(*parts, *zones, send_sems, recv_sems, after)
    return out[:ns], out[ns:]


def _grad_finish(sections, parts, far, chip, name):
    ns = len(sections)

    def body(chip_ref, *refs):
        p_refs, b_refs, g_refs = refs[:ns], refs[ns:2 * ns], refs[2 * ns:]
        for s in range(ns):
            g = p_refs[s][0].astype(F32)
            for j in range(3):
                g = g + b_refs[s][j].astype(F32)
            g_refs[s][...] = g

    half = [(rows // 2, cols) for _, rows, cols in sections]
    return pl.pallas_call(
        body, name=name,
        out_shape=tuple(jax.ShapeDtypeStruct((rows, cols), F32) for _, rows, cols in sections),
        grid_spec=pltpu.PrefetchScalarGridSpec(
            num_scalar_prefetch=1, grid=(2,),
            in_specs=[pl.BlockSpec((1, r, c), lambda i, chip: (chip[0], i, 0)) for r, c in half]
            + [pl.BlockSpec((3, r, c), lambda i, chip: (0, i, 0)) for r, c in half],
            out_specs=tuple(pl.BlockSpec((r, c), lambda i, chip: (i, 0)) for r, c in half)),
        compiler_params=_params(dimension_semantics=("parallel",)),
    )(chip, *parts, *far)


def _sum_devices(parts, rows, name):
    cols = parts.shape[1]
    tr = rows // 2

    def body(*refs):
        s = refs[0][...].astype(F32)
        for d in range(1, N_DEV):
            s = s + refs[d][...].astype(F32)
        refs[N_DEV][...] = s

    return pl.pallas_call(
        body, name=name,
        out_shape=jax.ShapeDtypeStruct((rows, cols), F32),
        grid=(2,),
        in_specs=[pl.BlockSpec((tr, cols), lambda i, d=d: (2 * d + i, 0)) for d in range(N_DEV)],
        out_specs=pl.BlockSpec((tr, cols), lambda i: (i, 0)),
        compiler_params=_params(dimension_semantics=("parallel",)),
    )(*([parts] * N_DEV))


def _adamw(w, g, m, v, name):
    rows, cols = w.shape
    tr = rows
    while tr * cols * 4 > (1 << 20) and tr % 16 == 0:
        tr //= 2
    c1 = 1.0 / (1.0 - ADAM_B1 ** ADAM_STEP)
    c2 = 1.0 / (1.0 - ADAM_B2 ** ADAM_STEP)

    def body(w_ref, g_ref, m_ref, v_ref, d_ref, nm_ref, nv_ref):
        gv = g_ref[...]
        nm = ADAM_B1 * m_ref[...] + (1.0 - ADAM_B1) * gv
        nv = ADAM_B2 * v_ref[...] + (1.0 - ADAM_B2) * (gv * gv)
        nm_ref[...] = nm
        nv_ref[...] = nv
        d_ref[...] = (-ADAM_LR) * ((nm * c1) / (jnp.sqrt(nv * c2) + ADAM_EPS) + ADAM_WD * w_ref[...])

    spec = pl.BlockSpec((tr, cols), lambda i: (i, 0))
    shape = jax.ShapeDtypeStruct((rows, cols), F32)
    return pl.pallas_call(
        body, name=name,
        out_shape=(shape, shape, shape),
        grid=(rows // tr,),
        in_specs=[spec] * 4, out_specs=(spec,) * 3,
        compiler_params=_params(dimension_semantics=("parallel",)),
    )(w, g, m, v)


NAMES = ("ln1_g", "w_in", "b_in", "rpb", "w_att_o", "conv_w", "conv_b", "w_rg_a", "b_rg_a", "w_rg_i",
         "b_rg_i", "lru_lambda", "w_rec_o", "w_out", "ln2_g", "w_ff1", "w_ff2", "lnf_g")
TRANSPOSED = {"w_in": "w_in_t", "w_att_o": "w_att_o_t", "w_ff1": "w_ff1_t"}
ROW_SHARDED = ("w_rec_o", "w_out", "w_ff2")
REPLICATED = (("ln1_g", (1, D)), ("b_in", (1, D_IN)), ("rpb", (N_HEADS * N_RPB_R, N_RPB_C)),
              ("conv_b", (1, D_REC)), ("w_rg_a", (2 * N_REC_BLOCKS * REC_BLOCK, REC_BLOCK)),
              ("w_rg_i", (2 * N_REC_BLOCKS * REC_BLOCK, REC_BLOCK)), ("ln2_g", (1, D)), ("lnf_g", (1, D)))
GATE_BLOCKS = ("w_rg_a", "w_rg_i")
SMALL_ROWS = 112


def _chan_bits(vectors):
    chan = jnp.concatenate(vectors, axis=0)
    bits = lax.bitcast_convert_type(chan, BF16).reshape(-1)
    return jnp.pad(bits, (0, CHAN_BLOCK_ROWS * D - bits.shape[0])).reshape(CHAN_BLOCK_ROWS, D)


def _chan_from_bits(gathered):
    bits = gathered.reshape(N_DEV, CHAN_BLOCK_ROWS * D)[:, :2 * N_CHAN_ROWS * LANES]
    chan = lax.bitcast_convert_type(bits.reshape(N_DEV, N_CHAN_ROWS, LANES, 2), F32)
    return chan.transpose(1, 0, 2).reshape(N_CHAN_ROWS, D)


def kernel(x, ln1_g, w_in, b_in, rpb, w_att_o, conv_w, conv_b, w_rg_a, b_rg_a, w_rg_i, b_rg_i, lru_lambda, w_rec_o, w_out, ln2_g, w_ff1, w_ff2, lnf_g, loss_target, m_ln1_g, m_w_in, m_b_in, m_rpb, m_w_att_o, m_conv_w, m_conv_b, m_w_rg_a, m_b_rg_a, m_w_rg_i, m_b_rg_i, m_lru_lambda, m_w_rec_o, m_w_out, m_ln2_g, m_w_ff1, m_w_ff2, m_lnf_g, v_ln1_g, v_w_in, v_b_in, v_rpb, v_w_att_o, v_conv_w, v_conv_b, v_w_rg_a, v_b_rg_a, v_w_rg_i, v_b_rg_i, v_lru_lambda, v_w_rec_o, v_w_out, v_ln2_g, v_w_ff1, v_w_ff2, v_lnf_g):
    w = dict(zip(NAMES, (ln1_g, w_in, b_in, rpb, w_att_o, conv_w, conv_b, w_rg_a, b_rg_a, w_rg_i,
                         b_rg_i, lru_lambda, w_rec_o, w_out, ln2_g, w_ff1, w_ff2, lnf_g)))
    m = dict(zip(NAMES, (m_ln1_g, m_w_in, m_b_in, m_rpb, m_w_att_o, m_conv_w, m_conv_b, m_w_rg_a,
                         m_b_rg_a, m_w_rg_i, m_b_rg_i, m_lru_lambda, m_w_rec_o, m_w_out, m_ln2_g,
                         m_w_ff1, m_w_ff2, m_lnf_g)))
    v = dict(zip(NAMES, (v_ln1_g, v_w_in, v_b_in, v_rpb, v_w_att_o, v_conv_w, v_conv_b, v_w_rg_a,
                         v_b_rg_a, v_w_rg_i, v_b_rg_i, v_lru_lambda, v_w_rec_o, v_w_out, v_ln2_g,
                         v_w_ff1, v_w_ff2, v_lnf_g)))
    xi, yi, ci = _position()

    shard = {t: w[n][0].T.astype(BF16) for n, t in TRANSPOSED.items()}
    shard.update({n: w[n][0].astype(BF16) for n in ROW_SHARDED})
    shard["chan"] = _chan_bits([w[n][0] for n, _ in CHAN])
    first, later = ("w_in_t", "chan"), ("w_rec_o", "w_out", "w_ff1_t", "w_ff2", "w_att_o_t")
    *gathered, done = _all_gather([shard[n] for n in first], "weight_all_gather")
    p = dict(zip(first, gathered))
    send_sems, recv_sems, sent, zones, token = _gather_start([shard[n] for n in later], done,
                                                             "weight_gather_start")

    def late_weights(after):
        landed = _gather_wait(send_sems, recv_sems, sent, zones, after, "weight_gather_wait")
        return dict(zip(later, _gather_pass_on([shard[n].shape[0] for n in later], landed,
                                               "weight_gather_pass_on")))

    chan = _chan_from_bits(p.pop("chan"))
    r0 = 0
    for n, rows in CHAN:
        p[n] = chan[r0:r0 + rows]
        r0 += rows
    p.update(ln1_g=w["ln1_g"], b_in=w["b_in"] + token[0, 0], rpb=w["rpb"][0], conv_b=w["conv_b"],
             w_rg_a=w["w_rg_a"][0], w_rg_i=w["w_rg_i"][0], ln2_g=w["ln2_g"],
             lnf_g=w["lnf_g"].reshape(1, D))

    core = jnp.reshape(ci, (1,)).astype(jnp.int32)
    chip = jnp.reshape(2 * xi + yi, (1,)).astype(jnp.int32)
    early_sections, late_sections = SECTIONS[1:], SECTIONS[:1]
    in_flight = {}

    def reduce_early(grads):
        chan_g = jnp.concatenate([grads[n] for n, _ in CHAN], axis=0)
        chan_g = chan_g.reshape(N_CHAN_ROWS, N_DEV, LANES).transpose(1, 0, 2).astype(BF16)
        chan_g = jnp.pad(chan_g.reshape(N_DEV, -1), ((0, 0), (0, CHAN_BLOCK_ROWS * D - N_CHAN_ROWS * LANES)))
        grads["chan"] = chan_g.reshape(N_DEV * CHAN_BLOCK_ROWS, D)
        sect = [grads[n] for n, _, _ in early_sections]
        got = _pair_exchange(early_sections, sect, "grad_pair_exchange_early")
        parts = _pair_add(early_sections, sect, got, core, "grad_pair_add_early")
        in_flight["early"] = _chip_exchange_start(early_sections, parts, "grad_chip_exchange_start")
        return in_flight["early"][-1][0, 0]

    loss_part, grad_x, grads = _local_step(x[0], loss_target[0], p, late_weights, reduce_early)
    sect = [grads[n] for n, _, _ in late_sections]
    got = _pair_exchange(late_sections, sect, "grad_pair_exchange_late")
    late_parts = _pair_add(late_sections, sect, got, core, "grad_pair_add_late")
    in_flight["late"] = _chip_exchange_start(late_sections, late_parts, "grad_chip_exchange_start_late")

    flat = jnp.concatenate([grads[n].reshape(-1) for n, _ in REPLICATED if n not in GATE_BLOCKS]
                           + [loss_part.reshape(-1) + in_flight["late"][-1][0, 0]])
    n_small = flat.shape[0]
    flat = jnp.pad(flat, (0, SMALL_ROWS * LANES - n_small)).reshape(SMALL_ROWS, LANES)
    gates = jnp.concatenate([grads[n].reshape(-1, LANES) for n in GATE_BLOCKS], axis=0).astype(BF16)
    small_parts, gate_parts, _ = _all_gather([flat, gates], "small_grad_all_gather")
    small = _sum_devices(small_parts, SMALL_ROWS, "small_grad_sum").reshape(-1)
    gate_sum = _sum_devices(gate_parts, gates.shape[0], "gate_block_grad_sum")
    loss = small[n_small - 1]

    g, delta, new_m, new_v = {}, {}, {}, {}

    def update(n, g2, shape2):
        d2, m2, v2 = _adamw(w[n].reshape(shape2), g2, m[n].reshape(shape2), v[n].reshape(shape2),
                            "adamw_" + n)
        g[n], delta[n], new_m[n], new_v[n] = (a.reshape(w[n].shape) for a in (g2, d2, m2, v2))

    o = 0
    for n, shape2 in REPLICATED:
        if n in GATE_BLOCKS:
            k, rows = GATE_BLOCKS.index(n), gate_sum.shape[0] // len(GATE_BLOCKS)
            update(n, gate_sum[k * rows:(k + 1) * rows].reshape(shape2), shape2)
        else:
            size = shape2[0] * shape2[1]
            update(n, small[o:o + size].reshape(shape2), shape2)
            o += size

    def finish(group, sections, after, name):
        send_sems, recv_sems, parts, zones, _ = in_flight[group]
        parts, far = _chip_exchange_wait(sections, send_sems, recv_sems, parts, zones, after,
                                         "grad_chip_exchange_wait_" + name)
        return dict(zip((n for n, _, _ in sections),
                        _grad_finish(sections, parts, far, chip, "grad_finish_" + name)))

    summed = finish("early", early_sections, delta["lnf_g"], "early")
    for n in ROW_SHARDED:
        update(n, summed[n], summed[n].shape)
    for n, t in TRANSPOSED.items():
        if t in summed:
            update(n, summed[t].T, summed[t].shape[::-1])
    chan_back = summed["chan"].reshape(-1)[:N_CHAN_ROWS * LANES].reshape(N_CHAN_ROWS, LANES)
    r0 = 0
    for n, rows in CHAN:
        update(n, chan_back[r0:r0 + rows], (rows, LANES))
        r0 += rows
    summed = finish("late", late_sections, delta["lru_lambda"], "late")
    update("w_in", summed["w_in_t"].T, summed["w_in_t"].shape[::-1])

    return (loss, grad_x[None], *[g[n] for n in NAMES], *[delta[n] for n in NAMES],
            *[new_m[n] for n in NAMES], *[new_v[n] for n in NAMES])
```

```python
import math

import numpy as np
import jax
import jax.numpy as jnp
from jax import lax
from jax.experimental import pallas as pl
from jax.experimental.pallas import tpu as pltpu

F32 = jnp.float32
BF16 = jnp.bfloat16

T = 2048
D = 1024
D_ATT = 512
D_REC = 1024
D_FF = 4096
D_IN = 5632
N_HEADS = 8
DH = 64
GRID_W = 64
ROWS = T // GRID_W
WIN_H = 8
WIN_W = 16
KWIN = WIN_H * GRID_W
N_RPB_R = 2 * WIN_H - 1
N_RPB_C = 2 * WIN_W - 1
N_REC_BLOCKS = 16
REC_BLOCK = 64
CG = 128
N_CG = D_REC // CG
LRU_C = 8.0
EPS = 1e-6
N_DEV = 8
N_CHIPS = 4
LANES = 128

ADAM_LR = 0.001
ADAM_B1 = 0.9
ADAM_B2 = 0.999
ADAM_EPS = 1e-08
ADAM_WD = 0.01
ADAM_STEP = 10

MESH_AXES = ("x", "y", "c")
VMEM_LIMIT = 56 * 1024 * 1024

TILE = 512
DZ_ARRAYS = ((0, 3, 1), (3, 4, 2), (7, 4, 2))
N_DZ_TILES = D_IN // TILE


def _params(**kw):
    return pltpu.CompilerParams(vmem_limit_bytes=VMEM_LIMIT, **kw)


def _att_tables():
    rq = np.arange(2 * GRID_W) % GRID_W
    kc = np.arange(KWIN) % GRID_W
    win_start = np.clip(rq - WIN_W // 2, 0, GRID_W - WIN_W)
    valid = (kc[None, :] >= win_start[:, None]) & (kc[None, :] < win_start[:, None] + WIN_W)
    return valid.astype(np.float32), _pair_mask()


def _pair_mask():
    half = np.arange(2 * DH) // DH
    return (half[:, None] == half[None, :]).astype(np.float32)


def _dup_table():
    return np.concatenate([np.eye(REC_BLOCK, dtype=np.float32)] * 2, axis=1)


def _sigmoid(x):
    return 0.5 * jnp.tanh(0.5 * x) + 0.5


def _softplus(x):
    return jnp.maximum(x, 0.0) + jnp.log(1.0 + jnp.exp(-jnp.abs(x)))


def _one_minus_square(log_a, a):
    x = 2.0 * log_a
    series = -x * (1.0 + x * (0.5 + x * (1.0 / 6.0)))
    return jnp.where(x > -0.02, series, 1.0 - a * a)


_GELU_C = math.sqrt(2.0 / math.pi)


def _gelu_and_grad(x):
    x2 = x * x
    inner = _GELU_C * (x + 0.044715 * x * x2)
    t = jnp.tanh(inner)
    g = 0.5 * x * (1.0 + t)
    dg = 0.5 * (1.0 + t) + 0.5 * x * (1.0 - t * t) * _GELU_C * (1.0 + 3.0 * 0.044715 * x2)
    return g, dg


def _dot(a, b):
    return jnp.dot(a, b, preferred_element_type=F32)


def _dot_nt(a, b):
    return lax.dot_general(a, b, (((1,), (1,)), ((), ())), preferred_element_type=F32)


def _dot_tn(a, b):
    return lax.dot_general(a, b, (((0,), (0,)), ((), ())), preferred_element_type=F32)


def _dot_exact(a, b):
    return jnp.dot(a, b, precision=lax.Precision.HIGHEST, preferred_element_type=F32)


def _shift_rows(x, s):
    n = x.shape[0]
    rows = lax.broadcasted_iota(jnp.int32, x.shape, 0)
    y = pltpu.roll(x, s % n, 0)
    if s > 0:
        return jnp.where(rows >= s, y, 0.0)
    return jnp.where(rows < n + s, y, 0.0)


def _rms_bwd(dh, xh, r, g):
    dxh = dh * g
    return r * (dxh - xh * jnp.mean(dxh * xh, axis=-1, keepdims=True))


def _matmul(a, b, mode, out_dtype, name, tm=512, tn=1024, tk=2048):
    if mode == "nn":
        (m, k), (k2, n) = a.shape, b.shape
    elif mode == "nt":
        (m, k), (n, k2) = a.shape, b.shape
    else:
        (k, m), (k2, n) = a.shape, b.shape
    assert k == k2
    tm, tn, tk = min(tm, m), min(tn, n), min(tk, k)
    assert m % tm == 0 and n % tn == 0 and k % tk == 0
    nk = k // tk
    dot = {"nn": _dot, "nt": _dot_nt, "tn": _dot_tn}[mode]

    def body(a_ref, b_ref, o_ref, acc):
        kk = pl.program_id(2)
        part = dot(a_ref[...].astype(BF16), b_ref[...].astype(BF16))
        if nk == 1:
            o_ref[...] = part.astype(out_dtype)
            return

        @pl.when(kk == 0)
        def _():
            acc[...] = part

        @pl.when(kk > 0)
        def _():
            acc[...] += part

        @pl.when(kk == nk - 1)
        def _():
            o_ref[...] = acc[...].astype(out_dtype)

    if mode == "tn":
        a_spec = pl.BlockSpec((tk, tm), lambda i, j, kk: (kk, i))
    else:
        a_spec = pl.BlockSpec((tm, tk), lambda i, j, kk: (i, kk))
    if mode == "nt":
        b_spec = pl.BlockSpec((tn, tk), lambda i, j, kk: (j, kk))
    else:
        b_spec = pl.BlockSpec((tk, tn), lambda i, j, kk: (kk, j))
    return pl.pallas_call(
        body, name=name,
        out_shape=jax.ShapeDtypeStruct((m, n), out_dtype),
        grid=(m // tm, n // tn, nk),
        in_specs=[a_spec, b_spec],
        out_specs=pl.BlockSpec((tm, tn), lambda i, j, kk: (i, j)),
        scratch_shapes=[pltpu.VMEM((tm, tn) if nk > 1 else (8, LANES), F32)],
        compiler_params=_params(dimension_semantics=("parallel", "parallel", "arbitrary")),
    )(a, b)


def _in_proj(x, g1, w_in_t, b_in):
    tm = 1024

    def body(x_ref, g_ref, w_ref, b_ref, qkv_ref, uy_ref, gg_ref, h_ref, h_scr):
        j = pl.program_id(1)

        @pl.when(j == 0)
        def _():
            xv = x_ref[...]
            r = lax.rsqrt(jnp.mean(xv * xv, axis=-1, keepdims=True) + EPS)
            h = ((xv * r) * g_ref[...]).astype(BF16)
            h_scr[...] = h
            h_ref[...] = h

        z = _dot_nt(h_scr[...], w_ref[...]) + b_ref[...]

        @pl.when(j < 3)
        def _():
            qkv_ref[...] = z.astype(BF16)

        @pl.when((j >= 3) & (j < 7))
        def _():
            uy_ref[...] = z

        @pl.when(j >= 7)
        def _():
            gg_ref[...] = z

    return pl.pallas_call(
        body, name="in_proj",
        out_shape=(jax.ShapeDtypeStruct((T, 3 * D_ATT), BF16),
                   jax.ShapeDtypeStruct((T, 2 * D_REC), F32),
                   jax.ShapeDtypeStruct((T, 2 * D), F32),
                   jax.ShapeDtypeStruct((T, D), BF16)),
        grid=(T // tm, N_DZ_TILES),
        in_specs=[pl.BlockSpec((tm, D), lambda i, j: (i, 0)),
                  pl.BlockSpec((1, D), lambda i, j: (0, 0)),
                  pl.BlockSpec((TILE, D), lambda i, j: (j, 0)),
                  pl.BlockSpec((1, TILE), lambda i, j: (0, j))],
        out_specs=(pl.BlockSpec((tm, TILE), lambda i, j: (i, jnp.minimum(j, 2))),
                   pl.BlockSpec((tm, TILE), lambda i, j: (i, jnp.clip(j - 3, 0, 3))),
                   pl.BlockSpec((tm, TILE), lambda i, j: (i, jnp.clip(j - 7, 0, 3))),
                   pl.BlockSpec((tm, D), lambda i, j: (i, 0))),
        scratch_shapes=[pltpu.VMEM((tm, D), BF16)],
        compiler_params=_params(dimension_semantics=("parallel", "arbitrary")),
    )(x, g1, w_in_t, b_in)


def _dz_specs(rows, tile_of, row_of):
    def spec(off, n, per_plane):
        def index(*ids):
            t = jnp.clip(tile_of(*ids) - off, 0, n - 1)
            return (t // per_plane, row_of(*ids), t % per_plane)
        return pl.BlockSpec((1, rows, TILE), index)
    return [spec(off, n, per) for off, n, per in DZ_ARRAYS]


def _dh_norm1_bwd(dz, w_in_t, x, g1, dx1):
    tm = 1024

    def body(*refs):
        seg_refs = refs[:3]
        w_ref, x_ref, g_ref, dx1_ref, gx_ref, dg_ref, acc = refs[3:]
        i, kk = pl.program_id(0), pl.program_id(1)

        @pl.when(kk == 0)
        def _():
            acc[...] = jnp.zeros_like(acc)

        for s, (off, n, _) in enumerate(DZ_ARRAYS):
            @pl.when((kk >= off) & (kk < off + n))
            def _(s=s):
                acc[...] += _dot(seg_refs[s][0], w_ref[...])

        @pl.when((i == 0) & (kk == 0))
        def _():
            dg_ref[...] = jnp.zeros_like(dg_ref)

        @pl.when(kk == N_DZ_TILES - 1)
        def _():
            xv = x_ref[...]
            r = lax.rsqrt(jnp.mean(xv * xv, axis=-1, keepdims=True) + EPS)
            xh = xv * r
            dh = acc[...]
            dg_ref[...] += jnp.sum(dh * xh, axis=0, keepdims=True)
            gx_ref[...] = dx1_ref[...] + _rms_bwd(dh, xh, r, g_ref[...])

    tok = pl.BlockSpec((tm, D), lambda i, j: (i, 0))
    vec = pl.BlockSpec((1, D), lambda i, j: (0, 0))
    return pl.pallas_call(
        body, name="dh_norm1_bwd",
        out_shape=(jax.ShapeDtypeStruct((T, D), F32), jax.ShapeDtypeStruct((1, D), F32)),
        grid=(T // tm, N_DZ_TILES),
        in_specs=_dz_specs(tm, lambda i, j: j, lambda i, j: i)
        + [pl.BlockSpec((TILE, D), lambda i, j: (j, 0)), tok, vec, tok],
        out_specs=(tok, vec),
        scratch_shapes=[pltpu.VMEM((tm, D), F32)],
        compiler_params=_params(dimension_semantics=("arbitrary", "arbitrary")),
    )(*dz, w_in_t, x, g1, dx1)


def _grad_w_in(dz, h):
    def body(*refs):
        seg_refs = refs[:3]
        h_ref, gw_ref, gb_ref = refs[3:]
        j = pl.program_id(0)

        for s, (off, n, _) in enumerate(DZ_ARRAYS):
            @pl.when((j >= off) & (j < off + n))
            def _(s=s):
                a = seg_refs[s][0]
                gw_ref[...] = _dot_tn(a, h_ref[...]).astype(BF16)
                gb_ref[...] = jnp.sum(a.astype(F32), axis=0, keepdims=True)

    return pl.pallas_call(
        body, name="grad_w_in",
        out_shape=(jax.ShapeDtypeStruct((D_IN, D), BF16), jax.ShapeDtypeStruct((1, D_IN), F32)),
        grid=(N_DZ_TILES,),
        in_specs=_dz_specs(T, lambda j: j, lambda j: 0) + [pl.BlockSpec((T, D), lambda j: (0, 0))],
        out_specs=(pl.BlockSpec((TILE, D), lambda j: (j, 0)), pl.BlockSpec((1, TILE), lambda j: (0, j))),
        compiler_params=_params(dimension_semantics=("parallel",)),
    )(*dz, h)


def _rpb_rows(rpb):
    padded = jnp.pad(rpb, ((0, 0), (0, 0), (0, GRID_W - N_RPB_C)))
    rows = [padded[:, WIN_H - 1 - oi: 2 * WIN_H - 1 - oi].reshape(N_HEADS // 2, 2, KWIN)
            for oi in range(WIN_H)]
    return jnp.stack(rows, axis=0)


SKEW = KWIN - (WIN_W - 1)


MASKED = -1e30


def _bias_tiles(rows_ref, valid, bias_s):
    for oi in range(WIN_H):
        for hh in range(2):
            row = jnp.broadcast_to(rows_ref[oi, 0, hh:hh + 1, :], (GRID_W, KWIN))
            tile = pltpu.roll(row, SKEW, 1, stride=1, stride_axis=0)
            bias_s[oi, hh * GRID_W:(hh + 1) * GRID_W, :] = jnp.where(valid[:GRID_W], tile, MASKED)


def _bias_tile_grads(gb_s, flip, out_ref):
    for oi in range(WIN_H):
        for hh in range(2):
            g = _dot_exact(flip, gb_s[oi, hh * GRID_W:(hh + 1) * GRID_W, :])
            back = pltpu.roll(g, KWIN - (GRID_W - WIN_W), 1, stride=1, stride_axis=0)
            out_ref[0, oi, hh:hh + 1, :] = jnp.sum(back, axis=0, keepdims=True)


def _rpb_fold(row_grads):
    g = row_grads.transpose(1, 0, 2, 3).reshape(WIN_H, N_HEADS, WIN_H, GRID_W)
    g = g.transpose(0, 2, 1, 3)

    def body(g_ref, o_ref):
        for dr in range(N_RPB_R):
            terms = [g_ref[oi, i] for oi in range(WIN_H) for i in range(WIN_H) if i - oi + WIN_H - 1 == dr]
            acc = terms[0]
            for term in terms[1:]:
                acc = acc + term
            o_ref[dr] = acc

    out = pl.pallas_call(
        body, name="rpb_fold",
        out_shape=jax.ShapeDtypeStruct((N_RPB_R, N_HEADS, GRID_W), F32),
    )(g)
    return out.transpose(1, 0, 2)[:, :, :N_RPB_C]


def _att_scores(q_ref, k_ref, bias_ref, hmask, r):
    rs = jnp.clip(r - WIN_H // 2, 0, ROWS - WIN_H)
    oi = r - rs
    q0 = pl.multiple_of(r * GRID_W, GRID_W)
    k0 = pl.multiple_of(rs * GRID_W, GRID_W)
    q_r = q_ref[pl.ds(q0, GRID_W), :] * (DH ** -0.5)
    q2 = jnp.where(hmask, jnp.concatenate([q_r, q_r], axis=0), jnp.zeros((), BF16))
    kw = k_ref[pl.ds(k0, KWIN), :]
    s = _dot_nt(q2, kw) + bias_ref[oi]
    e = jnp.exp(s - jnp.max(s, axis=-1, keepdims=True))
    return e, 1.0 / jnp.sum(e, axis=-1, keepdims=True), q2, kw, q0, k0, oi


def _att_fwd(qkv, bias_rows):
    valid_np, hmask_np = _att_tables()

    def body(q_ref, k_ref, v_ref, rows_ref, valid_ref, hmask_ref, o_ref, bias_s):
        valid = valid_ref[...] > 0.5
        hmask = hmask_ref[...] > 0.5
        first_head = lax.broadcasted_iota(jnp.int32, (GRID_W, 2 * DH), 1) < DH
        _bias_tiles(rows_ref, valid, bias_s)

        def row(r, carry):
            e, rl, _, _, q0, k0, _ = _att_scores(q_ref, k_ref, bias_s, hmask, r)
            o2 = _dot((e * rl).astype(BF16), v_ref[pl.ds(k0, KWIN), :])
            o_ref[pl.ds(q0, GRID_W), :] = jnp.where(first_head, o2[:GRID_W], o2[GRID_W:]).astype(BF16)
            return carry

        lax.fori_loop(0, ROWS, row, 0, unroll=4)

    col = lambda off: pl.BlockSpec((T, 2 * DH), lambda hp: (0, hp + off))
    return pl.pallas_call(
        body, name="att_fwd",
        out_shape=jax.ShapeDtypeStruct((T, D_ATT), BF16),
        grid=(N_HEADS // 2,),
        in_specs=[col(0), col(4), col(8),
                  pl.BlockSpec((WIN_H, 1, 2, KWIN), lambda hp: (0, hp, 0, 0)),
                  pl.BlockSpec((2 * GRID_W, KWIN), lambda hp: (0, 0)),
                  pl.BlockSpec((2 * DH, 2 * DH), lambda hp: (0, 0))],
        out_specs=pl.BlockSpec((T, 2 * DH), lambda hp: (0, hp)),
        scratch_shapes=[pltpu.VMEM((WIN_H, 2 * GRID_W, KWIN), F32)],
        compiler_params=_params(dimension_semantics=("parallel",)),
    )(qkv, qkv, qkv, bias_rows, jnp.asarray(valid_np), jnp.asarray(hmask_np))


def _att_bwd(qkv, bias_rows, datt, after):
    valid_np, hmask_np = _att_tables()

    def body(q_ref, k_ref, v_ref, do_ref, rows_ref, valid_ref, hmask_ref, flip_ref,
             dqkv_ref, grows_ref, dk_acc, dv_acc, bias_s, gb_s):
        valid = valid_ref[...] > 0.5
        hmask = hmask_ref[...] > 0.5
        first_head = lax.broadcasted_iota(jnp.int32, (GRID_W, 2 * DH), 1) < DH
        dk_acc[...] = jnp.zeros_like(dk_acc)
        dv_acc[...] = jnp.zeros_like(dv_acc)
        gb_s[...] = jnp.zeros_like(gb_s)
        _bias_tiles(rows_ref, valid, bias_s)

        def row(r, carry):
            e, rl, q2, kw, q0, k0, oi = _att_scores(q_ref, k_ref, bias_s, hmask, r)
            do_r = do_ref[pl.ds(q0, GRID_W), :]
            do2 = jnp.where(hmask, jnp.concatenate([do_r, do_r], axis=0), jnp.zeros((), BF16))
            vw = v_ref[pl.ds(k0, KWIN), :]
            p = e * rl
            dp = _dot_nt(do2, vw)
            ds = p * (dp - jnp.sum(dp * p, axis=-1, keepdims=True))
            p16 = p.astype(BF16)
            ds16 = ds.astype(BF16)
            dv_acc[pl.ds(k0, KWIN), :] += _dot_tn(p16, do2)
            dk_acc[pl.ds(k0, KWIN), :] += _dot_tn(ds16, q2)
            dq2 = _dot(ds16, kw) * (DH ** -0.5)
            dqkv_ref[0, pl.ds(q0, GRID_W), :] = jnp.where(first_head, dq2[:GRID_W], dq2[GRID_W:]).astype(BF16)
            gb_s[oi] += ds
            return carry

        lax.fori_loop(0, ROWS, row, 0, unroll=4)
        dqkv_ref[1] = dk_acc[...].astype(BF16)
        dqkv_ref[2] = dv_acc[...].astype(BF16)
        _bias_tile_grads(gb_s, flip_ref[...], grows_ref)

    col = lambda off: pl.BlockSpec((T, 2 * DH), lambda hp: (0, hp + off))
    tiles = pltpu.VMEM((WIN_H, 2 * GRID_W, KWIN), F32)
    return pl.pallas_call(
        body, name="att_bwd",
        out_shape=(jax.ShapeDtypeStruct((3, T, D_ATT), BF16),
                   jax.ShapeDtypeStruct((N_HEADS // 2, WIN_H, 2, KWIN), F32)),
        grid=(N_HEADS // 2,),
        in_specs=[col(0), col(4), col(8), col(0),
                  pl.BlockSpec((WIN_H, 1, 2, KWIN), lambda hp: (0, hp, 0, 0)),
                  pl.BlockSpec((2 * GRID_W, KWIN), lambda hp: (0, 0)),
                  pl.BlockSpec((2 * DH, 2 * DH), lambda hp: (0, 0)),
                  pl.BlockSpec((GRID_W, GRID_W), lambda hp: (0, 0))],
        out_specs=(pl.BlockSpec((3, T, 2 * DH), lambda hp: (0, 0, hp)),
                   pl.BlockSpec((1, WIN_H, 2, KWIN), lambda hp: (hp, 0, 0, 0))),
        scratch_shapes=[pltpu.VMEM((T, 2 * DH), F32), pltpu.VMEM((T, 2 * DH), F32), tiles, tiles],
        compiler_params=_params(dimension_semantics=("parallel",)),
    )(qkv, qkv, qkv, datt, bias_rows, jnp.asarray(valid_np) + after, jnp.asarray(hmask_np),
      jnp.asarray(np.eye(GRID_W, dtype=np.float32)[::-1].copy()))


def _conv_taps(up):
    return (_shift_rows(up, 2), _shift_rows(up, 1), up, _shift_rows(up, -1))


def _pair_block_diag(w_pair, dup, same_half):
    return jnp.where(same_half, _dot(w_pair.astype(BF16), dup), 0.0).astype(BF16)


def _gates(u, u16, wa, ba, wi, bi, lam):
    r = _sigmoid(_dot(u16, wa) + ba)
    ig = _sigmoid(_dot(u16, wi) + bi)
    sp = _softplus(-lam)
    log_a = (-LRU_C) * r * sp
    a = jnp.exp(log_a)
    mult2 = jnp.maximum(_one_minus_square(log_a, a), 0.0)
    return r, ig, sp, a, jnp.sqrt(mult2), mult2


SCAN_BLOCKS = 2


def _scans(jobs):
    c = jobs[0][0].shape[1]
    nblk = T // 8
    rows = lax.broadcasted_iota(jnp.int32, (8, c), 0)

    def block(a, b, reverse):
        for s in (1, 2, 4):
            if reverse:
                keep = rows < 8 - s
                a_s = jnp.where(keep, pltpu.roll(a, 8 - s, 0), 1.0)
                b_s = jnp.where(keep, pltpu.roll(b, 8 - s, 0), 0.0)
            else:
                keep = rows >= s
                a_s = jnp.where(keep, pltpu.roll(a, s, 0), 1.0)
                b_s = jnp.where(keep, pltpu.roll(b, s, 0), 0.0)
            b = a * b_s + b
            a = a * a_s
        return a, b

    def step(i, carry):
        out = []
        for (a_ref, b_ref, h_ref, reverse), h_prev in zip(jobs, carry):
            for u in range(SCAN_BLOCKS):
                blk = i * SCAN_BLOCKS + u
                if reverse:
                    blk = nblk - 1 - blk
                t0 = pl.multiple_of(blk * 8, 8)
                a, b = block(a_ref[pl.ds(t0, 8), :], b_ref[pl.ds(t0, 8), :], reverse)
                h = a * h_prev + b
                h_ref[pl.ds(t0, 8), :] = h
                h_prev = jnp.broadcast_to(h[0:1] if reverse else h[7:8], (8, c))
            out.append(h_prev)
        return tuple(out)

    lax.fori_loop(0, nblk // SCAN_BLOCKS, step, tuple(jnp.zeros((8, c), F32) for _ in jobs))


def _rec_specs():
    tok = lambda off: pl.BlockSpec((T, CG), lambda g: (0, g + off))
    per_ch = lambda rows: pl.BlockSpec((rows, CG), lambda g: (0, g))
    wspec = pl.BlockSpec((2, 1, CG, REC_BLOCK), lambda g: (0, g, 0, 0))
    const = lambda shape: pl.BlockSpec(shape, lambda g: (0, 0))
    return tok, per_ch, wspec, const


def _rec_fwd(uy, conv_w, conv_b, w_a, b_a, w_i, b_i, lam):
    tok, per_ch, wspec, const = _rec_specs()

    def body(up_ref, yb_ref, cw_ref, cb_ref, wa_ref, ba_ref, wi_ref, bi_ref, lam_ref, dup_ref, half_ref,
             hf_ref, hb_ref, yrec_ref, a_f, bx_f, a_b, bx_b):
        dup = dup_ref[...]
        same_half = half_ref[...] > 0.5
        taps = _conv_taps(up_ref[...])
        u = cb_ref[...]
        for j in range(4):
            u = u + taps[j] * cw_ref[j:j + 1, :]
        u16 = u.astype(BF16)
        for d, (a_s, bx_s) in enumerate(((a_f, bx_f), (a_b, bx_b))):
            wa = _pair_block_diag(wa_ref[d, 0], dup, same_half)
            wi = _pair_block_diag(wi_ref[d, 0], dup, same_half)
            _, ig, _, a, mult, _ = _gates(u, u16, wa, ba_ref[d:d + 1, :], wi, bi_ref[d:d + 1, :],
                                       lam_ref[d:d + 1, :])
            a_s[...] = a
            bx_s[...] = mult * (ig * u)
        _scans([(a_f, bx_f, hf_ref, False), (a_b, bx_b, hb_ref, True)])
        gelu, _ = _gelu_and_grad(yb_ref[...])
        yrec_ref[...] = ((hf_ref[...] + hb_ref[...]) * gelu).astype(BF16)

    return pl.pallas_call(
        body, name="rec_fwd",
        out_shape=(jax.ShapeDtypeStruct((T, D_REC), F32), jax.ShapeDtypeStruct((T, D_REC), F32),
                   jax.ShapeDtypeStruct((T, D_REC), BF16)),
        grid=(N_CG,),
        in_specs=[tok(0), tok(N_CG), per_ch(4), per_ch(1), wspec, per_ch(2), wspec, per_ch(2), per_ch(2),
                  const((REC_BLOCK, CG)), const((CG, CG))],
        out_specs=(tok(0), tok(0), tok(0)),
        scratch_shapes=[pltpu.VMEM((T, CG), F32)] * 4,
        compiler_params=_params(dimension_semantics=("parallel",)),
    )(uy, uy, conv_w, conv_b, w_a, b_a, w_i, b_i, lam,
      jnp.asarray(_dup_table(), BF16), jnp.asarray(_pair_mask()))


def _rec_bwd(uy, hf, hb, dyrec, conv_w, conv_b, w_a, b_a, w_i, b_i, lam):
    tok, per_ch, wspec, const = _rec_specs()

    def body(up_ref, yb_ref, hf_ref, hb_ref, dy_ref, cw_ref, cb_ref, wa_ref, ba_ref, wi_ref, bi_ref,
             lam_ref, dup_ref, dupt_ref, half_ref,
             duy_ref, dcw_ref, dcb_ref, dwa_ref, dba_ref, dwi_ref, dbi_ref, dlam_ref,
             a_s0, a_s1, dh_s, g_s0, g_s1):
        dup = dup_ref[...]
        dup_t = dupt_ref[...]
        same_half = half_ref[...] > 0.5
        taps = _conv_taps(up_ref[...])
        u = cb_ref[...]
        for j in range(4):
            u = u + taps[j] * cw_ref[j:j + 1, :]
        u16 = u.astype(BF16)
        gelu, dgelu = _gelu_and_grad(yb_ref[...])
        dy = dy_ref[...]
        duy_ref[1] = (dy * (hf_ref[...] + hb_ref[...]) * dgelu).astype(BF16)
        dh_s[...] = dy * gelu
        gate_values = []
        for d, a_s in enumerate((a_s0, a_s1)):
            wa = _pair_block_diag(wa_ref[d, 0], dup, same_half)
            wi = _pair_block_diag(wi_ref[d, 0], dup, same_half)
            lam_d = lam_ref[d:d + 1, :]
            r, ig, sp, a, mult, mult2 = _gates(u, u16, wa, ba_ref[d:d + 1, :], wi, bi_ref[d:d + 1, :], lam_d)
            a_s[...] = _shift_rows(a, 1 if d == 1 else -1)
            gate_values.append((wa, wi, lam_d, r, ig, sp, a, mult, mult2))
        _scans([(a_s0, dh_s, g_s0, True), (a_s1, dh_s, g_s1, False)])
        du = jnp.zeros((T, CG), F32)
        for d, g_s in enumerate((g_s0, g_s1)):
            reverse = d == 1
            wa, wi, lam_d, r, ig, sp, a, mult, mult2 = gate_values[d]
            g = g_s[...]
            h_prev = _shift_rows(hb_ref[...], -1) if reverse else _shift_rows(hf_ref[...], 1)
            da = g * h_prev
            dmult = g * (ig * u)
            dig = g * mult * u
            du = du + g * mult * ig
            dmult_dlog = jnp.where(mult2 > 0.0, -(a * a) * lax.rsqrt(mult2), 0.0)
            dlog_a = da * a + dmult * dmult_dlog
            dr = dlog_a * ((-LRU_C) * sp)
            dsp = jnp.sum(dlog_a * ((-LRU_C) * r), axis=0, keepdims=True)
            dlam_ref[d:d + 1, :] = dsp * (-_sigmoid(-lam_d))
            dga = dr * r * (1.0 - r)
            dgi = dig * ig * (1.0 - ig)
            dga16 = dga.astype(BF16)
            dgi16 = dgi.astype(BF16)
            du = du + _dot_nt(dga16, wa) + _dot_nt(dgi16, wi)
            dwa_ref[d, 0] = _dot_exact(jnp.where(same_half, _dot_tn(u16, dga16), 0.0), dup_t)
            dwi_ref[d, 0] = _dot_exact(jnp.where(same_half, _dot_tn(u16, dgi16), 0.0), dup_t)
            dba_ref[d:d + 1, :] = jnp.sum(dga, axis=0, keepdims=True)
            dbi_ref[d:d + 1, :] = jnp.sum(dgi, axis=0, keepdims=True)
        dcb_ref[...] = jnp.sum(du, axis=0, keepdims=True)
        for j in range(4):
            dcw_ref[j:j + 1, :] = jnp.sum(du * taps[j], axis=0, keepdims=True)
        dup_in = (_shift_rows(du, -2) * cw_ref[0:1, :] + _shift_rows(du, -1) * cw_ref[1:2, :]
                  + du * cw_ref[2:3, :] + _shift_rows(du, 1) * cw_ref[3:4, :])
        duy_ref[0] = dup_in.astype(BF16)

    wshape = jax.ShapeDtypeStruct((2, N_CG, CG, REC_BLOCK), F32)
    vec = lambda rows: jax.ShapeDtypeStruct((rows, D_REC), F32)
    dup_np = _dup_table()
    return pl.pallas_call(
        body, name="rec_bwd",
        out_shape=(jax.ShapeDtypeStruct((2, T, D_REC), BF16),
                   vec(4), vec(1), wshape, vec(2), wshape, vec(2), vec(2)),
        grid=(N_CG,),
        in_specs=[tok(0), tok(N_CG), tok(0), tok(0), tok(0),
                  per_ch(4), per_ch(1), wspec, per_ch(2), wspec, per_ch(2), per_ch(2),
                  const((REC_BLOCK, CG)), const((CG, REC_BLOCK)), const((CG, CG))],
        out_specs=(pl.BlockSpec((2, T, CG), lambda g: (0, 0, g)),
                   per_ch(4), per_ch(1), wspec, per_ch(2), wspec, per_ch(2), per_ch(2)),
        scratch_shapes=[pltpu.VMEM((T, CG), F32)] * 5,
        compiler_params=_params(dimension_semantics=("parallel",)),
    )(uy, uy, hf, hb, dyrec, conv_w, conv_b, w_a, b_a, w_i, b_i, lam,
      jnp.asarray(dup_np, BF16), jnp.asarray(dup_np.T.copy()), jnp.asarray(_pair_mask()))


TM_MIX = 256


def _mix_specs():
    tok = lambda width, blk=0: pl.BlockSpec((TM_MIX, width), lambda i: (i, blk))
    full = lambda shape: pl.BlockSpec(shape, lambda i: (0, 0))
    return tok, full


def _mix_fwd(x, att, yrec, gg, w_att_o_t, w_rec_o, w_out):
    tok, full = _mix_specs()

    def body(x_ref, att_ref, yr_ref, ga_ref, gr_ref, wao_ref, wro_ref, wo_ref, x1_ref, mixed_ref):
        y_att = _dot_nt(att_ref[...], wao_ref[...])
        y_rec = _dot(yr_ref[...], wro_ref[...])
        mixed = (_sigmoid(ga_ref[...]) * y_att + _sigmoid(gr_ref[...]) * y_rec).astype(BF16)
        mixed_ref[...] = mixed
        x1_ref[...] = x_ref[...] + _dot(mixed, wo_ref[...])

    return pl.pallas_call(
        body, name="mix_fwd",
        out_shape=(jax.ShapeDtypeStruct((T, D), F32), jax.ShapeDtypeStruct((T, D), BF16)),
        grid=(T // TM_MIX,),
        in_specs=[tok(D), tok(D_ATT), tok(D_REC), tok(D, 0), tok(D, 1),
                  full((D, D_ATT)), full((D_REC, D)), full((D, D))],
        out_specs=(tok(D), tok(D)),
        compiler_params=_params(dimension_semantics=("parallel",)),
    )(x, att, yrec, gg, gg, w_att_o_t, w_rec_o, w_out)


def _mix_bwd(dx1, att, yrec, gg, w_att_o_t, w_rec_o, w_out):
    tok, full = _mix_specs()

    def body(dx_ref, att_ref, yr_ref, ga_ref, gr_ref, wao_ref, wro_ref, wo_ref,
             dgg_ref, dya_ref, dyr_ref, datt_ref, dyrp_ref):
        dmixed = _dot_nt(dx_ref[...].astype(BF16), wo_ref[...])
        y_att = _dot_nt(att_ref[...], wao_ref[...])
        y_rec = _dot(yr_ref[...], wro_ref[...])
        sa = _sigmoid(ga_ref[...])
        sr = _sigmoid(gr_ref[...])
        dgg_ref[0] = (dmixed * y_att * sa * (1.0 - sa)).astype(BF16)
        dgg_ref[1] = (dmixed * y_rec * sr * (1.0 - sr)).astype(BF16)
        dya = (dmixed * sa).astype(BF16)
        dyr = (dmixed * sr).astype(BF16)
        dya_ref[...] = dya
        dyr_ref[...] = dyr
        datt_ref[...] = _dot(dya, wao_ref[...]).astype(BF16)
        dyrp_ref[...] = _dot_nt(dyr, wro_ref[...])

    return pl.pallas_call(
        body, name="mix_bwd",
        out_shape=(jax.ShapeDtypeStruct((2, T, D), BF16),
                   jax.ShapeDtypeStruct((T, D), BF16), jax.ShapeDtypeStruct((T, D), BF16),
                   jax.ShapeDtypeStruct((T, D_ATT), BF16), jax.ShapeDtypeStruct((T, D_REC), F32)),
        grid=(T // TM_MIX,),
        in_specs=[tok(D), tok(D_ATT), tok(D_REC), tok(D, 0), tok(D, 1),
                  full((D, D_ATT)), full((D_REC, D)), full((D, D))],
        out_specs=(pl.BlockSpec((2, TM_MIX, D), lambda i: (0, i, 0)),
                   tok(D), tok(D), tok(D_ATT), tok(D_REC)),
        compiler_params=_params(dimension_semantics=("parallel",)),
    )(dx1, att, yrec, gg, gg, w_att_o_t, w_rec_o, w_out)


TM_FFN = 256
FF_CHUNK = 1024


def _ffn_loss(x1, target, g2, gf, w_ff1_t, w_ff2):
    n_chunks = D_FF // FF_CHUNK

    def body(x1_ref, tg_ref, g2_ref, gf_ref, w1_hbm, w2_hbm,
             loss_ref, dx1_ref, h2_ref, act_ref, dpre_ref, dx2_ref, dg2_ref, dgf_ref,
             w1, w2, relu_s):
        i = pl.program_id(0)

        @pl.when(i == 0)
        def _():
            pltpu.sync_copy(w1_hbm, w1)
            pltpu.sync_copy(w2_hbm, w2)
            loss_ref[...] = jnp.zeros_like(loss_ref)
            dg2_ref[...] = jnp.zeros_like(dg2_ref)
            dgf_ref[...] = jnp.zeros_like(dgf_ref)

        x1v = x1_ref[...]
        r2 = lax.rsqrt(jnp.mean(x1v * x1v, axis=-1, keepdims=True) + EPS)
        xh2 = x1v * r2
        h2 = (xh2 * g2_ref[...]).astype(BF16)
        h2_ref[...] = h2
        x2 = x1v
        for c in range(n_chunks):
            ff = slice(c * FF_CHUNK, (c + 1) * FF_CHUNK)
            rl = jnp.maximum(_dot_nt(h2, w1[ff, :]), 0.0)
            relu_s[:, ff] = rl
            act = (rl * rl).astype(BF16)
            act_ref[:, ff] = act
            x2 = x2 + _dot(act, w2[ff, :])
        r3 = lax.rsqrt(jnp.mean(x2 * x2, axis=-1, keepdims=True) + EPS)
        xh3 = x2 * r3
        err = xh3 * gf_ref[...] - tg_ref[...]
        loss_ref[...] += 0.5 * jnp.sum(jnp.mean(err * err, axis=-1, keepdims=True))
        dy = err * (1.0 / D)
        dgf_ref[...] += jnp.sum(dy * xh3, axis=0, keepdims=True)
        dx2 = _rms_bwd(dy, xh3, r3, gf_ref[...])
        dx2_16 = dx2.astype(BF16)
        dx2_ref[...] = dx2_16
        dh2 = jnp.zeros((TM_FFN, D), F32)
        for c in range(n_chunks):
            ff = slice(c * FF_CHUNK, (c + 1) * FF_CHUNK)
            dpre = (_dot_nt(dx2_16, w2[ff, :]) * (2.0 * relu_s[:, ff])).astype(BF16)
            dpre_ref[:, ff] = dpre
            dh2 = dh2 + _dot(dpre, w1[ff, :])
        dg2_ref[...] += jnp.sum(dh2 * xh2, axis=0, keepdims=True)
        dx1_ref[...] = dx2 + _rms_bwd(dh2, xh2, r2, g2_ref[...])

    tok = lambda width: pl.BlockSpec((TM_FFN, width), lambda i: (i, 0))
    vec = pl.BlockSpec((1, D), lambda i: (0, 0))
    hbm = pl.BlockSpec(memory_space=pl.ANY)
    return pl.pallas_call(
        body, name="ffn_loss",
        out_shape=(jax.ShapeDtypeStruct((8, 128), F32), jax.ShapeDtypeStruct((T, D), F32),
                   jax.ShapeDtypeStruct((T, D), BF16), jax.ShapeDtypeStruct((T, D_FF), BF16),
                   jax.ShapeDtypeStruct((T, D_FF), BF16), jax.ShapeDtypeStruct((T, D), BF16),
                   jax.ShapeDtypeStruct((1, D), F32), jax.ShapeDtypeStruct((1, D), F32)),
        grid=(T // TM_FFN,),
        in_specs=[tok(D), tok(D), vec, vec, hbm, hbm],
        out_specs=(pl.BlockSpec((8, 128), lambda i: (0, 0)), tok(D), tok(D), tok(D_FF), tok(D_FF), tok(D),
                   vec, vec),
        scratch_shapes=[pltpu.VMEM((D_FF, D), BF16), pltpu.VMEM((D_FF, D), BF16),
                        pltpu.VMEM((TM_FFN, D_FF), F32)],
        compiler_params=_params(dimension_semantics=("arbitrary",)),
    )(x1, target, g2, gf, w_ff1_t, w_ff2)


def _local_step(x, target, p, late_weights, reduce_early):
    bias = _rpb_rows(p["rpb"])
    pairs = lambda w: w.reshape(2, N_CG, CG, REC_BLOCK)
    w_a, w_i = pairs(p["w_rg_a"]), pairs(p["w_rg_i"])
    rec_params = (p["conv_w"], p["conv_b"], w_a, p["b_rg_a"], w_i, p["b_rg_i"], p["lru_lambda"])

    qkv, uy, gg, h = _in_proj(x, p["ln1_g"], p["w_in_t"], p["b_in"])
    att = _att_fwd(qkv, bias)
    hf, hb, yrec = _rec_fwd(uy, *rec_params)
    p = {**p, **late_weights(yrec)}
    x1, mixed = _mix_fwd(x, att, yrec, gg, p["w_att_o_t"], p["w_rec_o"], p["w_out"])
    loss8, dx1, h2, act, dpre, dx2, g_ln2, g_lnf = _ffn_loss(
        x1, target, p["ln2_g"], p["lnf_g"], p["w_ff1_t"], p["w_ff2"])

    dgg, dya, dyr, datt, dyrp = _mix_bwd(dx1, att, yrec, gg, p["w_att_o_t"], p["w_rec_o"], p["w_out"])
    duy, g_cw, g_cb, g_wa, g_ba, g_wi, g_bi, g_lam = _rec_bwd(uy, hf, hb, dyrp, *rec_params)
    blocks = lambda g: g.reshape(2, N_REC_BLOCKS, REC_BLOCK, REC_BLOCK)
    grads = {
        "w_att_o_t": _matmul(dya, att, "tn", BF16, "g_w_att_o"),
        "conv_w": g_cw, "conv_b": g_cb, "w_rg_a": blocks(g_wa), "b_rg_a": g_ba,
        "w_rg_i": blocks(g_wi), "b_rg_i": g_bi, "lru_lambda": g_lam,
        "w_rec_o": _matmul(yrec, dyr, "tn", BF16, "g_w_rec_o"),
        "w_out": _matmul(mixed, dx1, "tn", BF16, "g_w_out"),
        "ln2_g": g_ln2,
        "w_ff1_t": _matmul(dpre, h2, "tn", BF16, "g_w_ff1"),
        "w_ff2": _matmul(act, dx2, "tn", BF16, "g_w_ff2"),
        "lnf_g": g_lnf,
    }
    after = reduce_early(grads)
    dqkv, gbias = _att_bwd(qkv, bias, datt, after)
    dz = (dqkv, duy, dgg)
    grad_x, g_ln1 = _dh_norm1_bwd(dz, p["w_in_t"], x, p["ln1_g"], dx1)
    g_w_in_t, g_b_in = _grad_w_in(dz, h)
    grads.update(ln1_g=g_ln1, w_in_t=g_w_in_t, b_in=g_b_in, rpb=_rpb_fold(gbias))
    return loss8[0:1, 0:1], grad_x, grads


MESH_ID = pl.DeviceIdType.MESH
ANY = pl.BlockSpec(memory_space=pl.ANY)

CHAN_BLOCK_ROWS = 32
GATE_ROWS = 2 * 2 * N_REC_BLOCKS * REC_BLOCK * REC_BLOCK // (N_DEV * D)
SECTIONS = (("w_in_t", 704, D), ("w_rec_o", 128, D), ("w_out", 128, D), ("w_ff1_t", 512, D),
            ("w_ff2", 512, D), ("chan", CHAN_BLOCK_ROWS, D), ("w_att_o_t", 128, D_ATT),
            ("gates", GATE_ROWS, D))
N_SEC = len(SECTIONS)
N_CHAN_ROWS = 10
CHAN = (("conv_w", 4), ("b_rg_a", 2), ("b_rg_i", 2), ("lru_lambda", 2))


def _position():
    return lax.axis_index("x"), lax.axis_index("y"), lax.axis_index("c")


def _other_chips(x, y):
    return [(1 - x, y), (x, 1 - y), (1 - x, 1 - y)]


def _block_of(ref, dev, rows):
    return ref.at[pl.ds(pl.multiple_of(dev * rows, 16), rows)]


def _all_gather(shards, name):
    ns = len(shards)

    def body(*refs):
        x_refs, out_refs, done_ref = refs[:ns], refs[ns:2 * ns], refs[2 * ns]
        send_sems, recv_sems, local_sems = refs[2 * ns + 1:]
        done_ref[0, 0] = 0.0
        x, y, c = _position()
        me, sibling = (x, y, c), (x, y, 1 - c)
        chips = _other_chips(x, y)

        def rows(s, px, py, pc):
            return _block_of(out_refs[s], 4 * px + 2 * py + pc, shards[s].shape[0])

        def copy(k, s, block, to, from_shard=False):
            return pltpu.make_async_remote_copy(
                src_ref=x_refs[s] if from_shard else rows(s, *block), dst_ref=rows(s, *block),
                send_sem=send_sems.at[k * ns + s], recv_sem=recv_sems.at[k * ns + s],
                device_id=to, device_id_type=MESH_ID)

        sections = range(ns)
        mine = [pltpu.make_async_copy(x_refs[s], rows(s, *me), local_sems.at[s]) for s in sections]
        first = [copy(0, s, me, sibling, True) for s in sections]
        first += [copy(1 + j, s, me, (*chip, c), True) for j, chip in enumerate(chips) for s in sections]
        for cp in mine + first:
            cp.start()
        passed = []
        for j, chip in enumerate(chips):
            for s in sections:
                copy(1 + j, s, (*chip, c), me).wait_recv()
                passed.append(copy(4 + j, s, (*chip, c), sibling))
                passed[-1].start()
        for s in sections:
            copy(0, s, sibling, me).wait_recv()
        for j, chip in enumerate(chips):
            for s in sections:
                copy(4 + j, s, (*chip, 1 - c), me).wait_recv()
        for cp in first + passed:
            cp.wait_send()
        for cp in mine:
            cp.wait()

    return pl.pallas_call(
        body, name=name,
        out_shape=tuple(jax.ShapeDtypeStruct((N_DEV * s.shape[0], s.shape[1]), s.dtype) for s in shards)
        + (jax.ShapeDtypeStruct((1, 1), F32),),
        in_specs=[ANY] * ns,
        out_specs=(ANY,) * ns + (pl.BlockSpec(memory_space=pltpu.SMEM),),
        scratch_shapes=[pltpu.SemaphoreType.DMA((7 * ns,)), pltpu.SemaphoreType.DMA((7 * ns,)),
                        pltpu.SemaphoreType.DMA((ns,))],
    )(*shards)


HBM = pl.BlockSpec(memory_space=pltpu.HBM)
SEM = pl.BlockSpec(memory_space=pltpu.SEMAPHORE)
EFFECT = pltpu.SideEffectType.DATAFLOW_SIDE_EFFECTING


def _in_hbm(a):
    return pltpu.with_memory_space_constraint(a, pltpu.HBM)


def _first_hop_copies(shards, x_refs, zones, send_sems, recv_sems):
    ns = len(shards)
    x, y, c = _position()
    targets = [(x, y, 1 - c)] + [(cx, cy, c) for cx, cy in _other_chips(x, y)]
    return [pltpu.make_async_remote_copy(
        src_ref=x_refs[s], dst_ref=_block_of(zones[s], 4 * x + 2 * y + c, shards[s].shape[0]),
        send_sem=send_sems.at[k * ns + s], recv_sem=recv_sems.at[k * ns + s],
        device_id=to, device_id_type=MESH_ID)
        for k, to in enumerate(targets) for s in range(ns)]


def _own_blocks_placed(shards, after):
    ns = len(shards)
    x, y, c = _position()
    me = jnp.reshape(4 * x + 2 * y + c, (1,)).astype(jnp.int32)
    shards = [*shards[:-1], shards[-1] + after.astype(shards[-1].dtype)]

    def body(me_ref, *refs):
        for s in range(ns):
            refs[ns + s][...] = refs[s][...]

    return pl.pallas_call(
        body, name="own_blocks_placed",
        out_shape=tuple(jax.ShapeDtypeStruct((N_DEV * s.shape[0], s.shape[1]), s.dtype) for s in shards),
        grid_spec=pltpu.PrefetchScalarGridSpec(
            num_scalar_prefetch=1, grid=(1,),
            in_specs=[pl.BlockSpec(s.shape, lambda i, me: (0, 0)) for s in shards],
            out_specs=tuple(pl.BlockSpec(s.shape, lambda i, me: (me[0], 0)) for s in shards)),
        compiler_params=_params(dimension_semantics=("arbitrary",)),
    )(me, *shards)


def _gather_start(shards, after, name):
    ns = len(shards)
    zones = _own_blocks_placed(shards, after)

    def body(*refs):
        for cp in _first_hop_copies(shards, refs[:ns], refs[ns:2 * ns], refs[2 * ns], refs[2 * ns + 1]):
            cp.start()
        refs[-1][...] = jnp.zeros_like(refs[-1])

    out = pl.pallas_call(
        body, name=name,
        out_shape=(pltpu.SemaphoreType.DMA((4 * ns,)), pltpu.SemaphoreType.DMA((4 * ns,)),
                   *[pltpu.HBM(a.shape, a.dtype) for a in (*shards, *zones)],
                   jax.ShapeDtypeStruct((8, LANES), F32)),
        in_specs=[HBM] * (2 * ns),
        out_specs=(SEM, SEM, *[HBM] * (2 * ns), pl.BlockSpec(memory_space=pltpu.VMEM)),
        input_output_aliases={i: 2 + i for i in range(2 * ns)},
        compiler_params=pltpu.CompilerParams(has_side_effects=EFFECT),
    )(*[_in_hbm(a) for a in shards], *[_in_hbm(a) for a in zones])
    return out[0], out[1], out[2:2 + ns], out[2 + ns:2 + 2 * ns], out[-1]


def _gather_wait(send_sems, recv_sems, shards, zones, after, name):
    ns = len(shards)

    def body(*refs):
        for cp in _first_hop_copies(shards, refs[:ns], refs[ns:2 * ns], refs[2 * ns], refs[2 * ns + 1]):
            cp.wait_send()
            cp.wait_recv()

    out = pl.pallas_call(
        body, name=name,
        out_shape=tuple(pltpu.HBM(a.shape, a.dtype) for a in (*shards, *zones)),
        in_specs=[HBM] * (2 * ns) + [SEM, SEM, ANY],
        out_specs=(HBM,) * (2 * ns),
        input_output_aliases={i: i for i in range(2 * ns)},
        compiler_params=pltpu.CompilerParams(has_side_effects=EFFECT),
    )(*shards, *zones, send_sems, recv_sems, after)
    return out[ns:]


def _gather_pass_on(rows, zones, name):
    ns = len(zones)

    def body(*refs):
        in_refs, out_refs = refs[:ns], refs[ns:2 * ns]
        send_sems, recv_sems = refs[2 * ns:]
        x, y, c = _position()
        copies = [pltpu.make_async_remote_copy(
            src_ref=_block_of(in_refs[s], 4 * cx + 2 * cy + c, rows[s]),
            dst_ref=_block_of(out_refs[s], 4 * cx + 2 * cy + c, rows[s]),
            send_sem=send_sems.at[j * ns + s], recv_sem=recv_sems.at[j * ns + s],
            device_id=(x, y, 1 - c), device_id_type=MESH_ID)
            for j, (cx, cy) in enumerate(_other_chips(x, y)) for s in range(ns)]
        for cp in copies:
            cp.start()
        for cp in copies:
            cp.wait_recv()
        for cp in copies:
            cp.wait_send()

    return pl.pallas_call(
        body, name=name,
        out_shape=tuple(jax.ShapeDtypeStruct(z.shape, z.dtype) for z in zones),
        in_specs=[ANY] * ns, out_specs=(ANY,) * ns,
        input_output_aliases={i: i for i in range(ns)},
        scratch_shapes=[pltpu.SemaphoreType.DMA((3 * ns,)), pltpu.SemaphoreType.DMA((3 * ns,))],
    )(*zones)


def _pair_exchange(sections, grads, name):
    ns = len(sections)

    def body(*refs):
        g_refs, land = refs[:ns], refs[ns:2 * ns]
        send_sems, recv_sems = refs[2 * ns:]
        x, y, c = _position()
        copies = [pltpu.make_async_remote_copy(
            src_ref=_block_of(g_refs[s], 2 * k + 1 - c, rows), dst_ref=land[s].at[k],
            send_sem=send_sems.at[k * ns + s], recv_sem=recv_sems.at[k * ns + s],
            device_id=(x, y, 1 - c), device_id_type=MESH_ID)
            for k in range(N_CHIPS) for s, (_, rows, _) in enumerate(sections)]
        for cp in copies:
            cp.start()
        for cp in copies:
            cp.wait_recv()
        for cp in copies:
            cp.wait_send()

    n = N_CHIPS * ns
    return pl.pallas_call(
        body, name=name,
        out_shape=tuple(jax.ShapeDtypeStruct((N_CHIPS, rows, cols), BF16) for _, rows, cols in sections),
        in_specs=[ANY] * ns, out_specs=(ANY,) * ns,
        scratch_shapes=[pltpu.SemaphoreType.DMA((n,)), pltpu.SemaphoreType.DMA((n,))],
    )(*grads)


def _pair_add(sections, grads, got, core, name):
    ns = len(sections)

    def body(core_ref, *refs):
        g_refs, got_refs, p_refs = refs[:ns], refs[ns:2 * ns], refs[2 * ns:]
        for s in range(ns):
            p_refs[s][0] = (g_refs[s][...].astype(F32) + got_refs[s][0].astype(F32)).astype(BF16)

    slot = [pl.BlockSpec((1, rows, cols), lambda k, c: (k, 0, 0)) for _, rows, cols in sections]
    return pl.pallas_call(
        body, name=name,
        out_shape=tuple(jax.ShapeDtypeStruct((N_CHIPS, rows, cols), BF16) for _, rows, cols in sections),
        grid_spec=pltpu.PrefetchScalarGridSpec(
            num_scalar_prefetch=1, grid=(N_CHIPS,),
            in_specs=[pl.BlockSpec((rows, cols), lambda k, c: (2 * k + c[0], 0)) for _, rows, cols in sections]
            + slot,
            out_specs=tuple(slot)),
        compiler_params=_params(dimension_semantics=("parallel",)),
    )(core, *grads, *got)


def _chip_copies(sections, p_refs, land, send_sems, recv_sems):
    ns = len(sections)
    x, y, c = _position()
    return [pltpu.make_async_remote_copy(
        src_ref=p_refs[s].at[2 * cx + cy], dst_ref=land[s].at[j],
        send_sem=send_sems.at[j * ns + s], recv_sem=recv_sems.at[j * ns + s],
        device_id=(cx, cy, c), device_id_type=MESH_ID)
        for j, (cx, cy) in enumerate(_other_chips(x, y)) for s in range(ns)]


def _chip_exchange(sections, parts, name):
    ns = len(sections)

    def body(*refs):
        copies = _chip_copies(sections, refs[:ns], refs[ns:2 * ns], *refs[2 * ns:])
        for cp in copies:
            cp.start()
        for cp in copies:
            cp.wait_recv()
        for cp in copies:
            cp.wait_send()

    n = 3 * ns
    return pl.pallas_call(
        body, name=name,
        out_shape=tuple(jax.ShapeDtypeStruct((3, rows, cols), BF16) for _, rows, cols in sections),
        in_specs=[ANY] * ns, out_specs=(ANY,) * ns,
        scratch_shapes=[pltpu.SemaphoreType.DMA((n,)), pltpu.SemaphoreType.DMA((n,))],
    )(*parts)


def _chip_exchange_start(sections, parts, name):
    ns = len(sections)

    def body(*refs):
        p_refs, land = refs[:ns], refs[ns:2 * ns]
        send_sems, recv_sems = refs[2 * ns], refs[2 * ns + 1]
        token = refs[-1]
        for cp in _chip_copies(sections, p_refs, land, send_sems, recv_sems):
            cp.start()
        token[...] = jnp.zeros_like(token)

    zones = [lax.empty((3, rows, cols), BF16) for _, rows, cols in sections]
    out = pl.pallas_call(
        body, name=name,
        out_shape=(pltpu.SemaphoreType.DMA((3 * ns,)), pltpu.SemaphoreType.DMA((3 * ns,)),
                   *[pltpu.HBM(a.shape, a.dtype) for a in parts], *[pltpu.HBM(a.shape, a.dtype) for a in zones],
                   jax.ShapeDtypeStruct((8, LANES), F32)),
        in_specs=[HBM] * (2 * ns),
        out_specs=(SEM, SEM, *[HBM] * (2 * ns), pl.BlockSpec(memory_space=pltpu.VMEM)),
        input_output_aliases={i: 2 + i for i in range(2 * ns)},
        compiler_params=pltpu.CompilerParams(has_side_effects=EFFECT),
    )(*[_in_hbm(a) for a in parts], *[_in_hbm(a) for a in zones])
    return out[0], out[1], out[2:2 + ns], out[2 + ns:2 + 2 * ns], out[-1]


def _chip_exchange_wait(sections, send_sems, recv_sems, parts, zones, after, name):
    ns = len(sections)

    def body(*refs):
        p_refs, land = refs[:ns], refs[ns:2 * ns]
        for cp in _chip_copies(sections, p_refs, land, refs[2 * ns], refs[2 * ns + 1]):
            cp.wait_send()
            cp.wait_recv()

    out = pl.pallas_call(
        body, name=name,
        out_shape=tuple(pltpu.HBM(a.shape, a.dtype) for a in (*parts, *zones)),
        in_specs=[HBM] * (2 * ns) + [SEM, SEM, ANY],
        out_specs=(HBM,) * (2 * ns),
        input_output_aliases={i: i for i in range(2 * ns)},
        compiler_params=pltpu.CompilerParams(has_side_effects=EFFECT),
    )(*parts, *zones, send_sems, recv_sems, after)
    return out[:ns], out[ns:]


def _grad_finish(sections, parts, far, chip, name):
    ns = len(sections)

    def body(chip_ref, *refs):
        p_refs, b_refs, g_refs = refs[:ns], refs[ns:2 * ns], refs[2 * ns:]
        for s in range(ns):
            g = p_refs[s][0].astype(F32)
            for j in range(3):
                g = g + b_refs[s][j].astype(F32)
            g_refs[s][...] = g

    half = [(rows // 2, cols) for _, rows, cols in sections]
    return pl.pallas_call(
        body, name=name,
        out_shape=tuple(jax.ShapeDtypeStruct((rows, cols), F32) for _, rows, cols in sections),
        grid_spec=pltpu.PrefetchScalarGridSpec(
            num_scalar_prefetch=1, grid=(2,),
            in_specs=[pl.BlockSpec((1, r, c), lambda i, chip: (chip[0], i, 0)) for r, c in half]
            + [pl.BlockSpec((3, r, c), lambda i, chip: (0, i, 0)) for r, c in half],
            out_specs=tuple(pl.BlockSpec((r, c), lambda i, chip: (i, 0)) for r, c in half)),
        compiler_params=_params(dimension_semantics=("parallel",)),
    )(chip, *parts, *far)


def _sum_devices(parts, rows, name):
    cols = parts.shape[1]
    tr = rows // 2

    def body(*refs):
        s = refs[0][...].astype(F32)
        for d in range(1, N_DEV):
            s = s + refs[d][...].astype(F32)
        refs[N_DEV][...] = s

    return pl.pallas_call(
        body, name=name,
        out_shape=jax.ShapeDtypeStruct((rows, cols), F32),
        grid=(2,),
        in_specs=[pl.BlockSpec((tr, cols), lambda i, d=d: (2 * d + i, 0)) for d in range(N_DEV)],
        out_specs=pl.BlockSpec((tr, cols), lambda i: (i, 0)),
        compiler_params=_params(dimension_semantics=("parallel",)),
    )(*([parts] * N_DEV))


def _adamw(w, g, m, v, name):
    rows, cols = w.shape
    tr = rows
    while tr * cols * 4 > (1 << 20) and tr % 16 == 0:
        tr //= 2
    c1 = 1.0 / (1.0 - ADAM_B1 ** ADAM_STEP)
    c2 = 1.0 / (1.0 - ADAM_B2 ** ADAM_STEP)

    def body(w_ref, g_ref, m_ref, v_ref, d_ref, nm_ref, nv_ref):
        gv = g_ref[...]
        nm = ADAM_B1 * m_ref[...] + (1.0 - ADAM_B1) * gv
        nv = ADAM_B2 * v_ref[...] + (1.0 - ADAM_B2) * (gv * gv)
        nm_ref[...] = nm
        nv_ref[...] = nv
        d_ref[...] = (-ADAM_LR) * ((nm * c1) / (jnp.sqrt(nv * c2) + ADAM_EPS) + ADAM_WD * w_ref[...])

    spec = pl.BlockSpec((tr, cols), lambda i: (i, 0))
    shape = jax.ShapeDtypeStruct((rows, cols), F32)
    return pl.pallas_call(
        body, name=name,
        out_shape=(shape, shape, shape),
        grid=(rows // tr,),
        in_specs=[spec] * 4, out_specs=(spec,) * 3,
        compiler_params=_params(dimension_semantics=("parallel",)),
    )(w, g, m, v)


NAMES = ("ln1_g", "w_in", "b_in", "rpb", "w_att_o", "conv_w", "conv_b", "w_rg_a", "b_rg_a", "w_rg_i",
         "b_rg_i", "lru_lambda", "w_rec_o", "w_out", "ln2_g", "w_ff1", "w_ff2", "lnf_g")
TRANSPOSED = {"w_in": "w_in_t", "w_att_o": "w_att_o_t", "w_ff1": "w_ff1_t"}
ROW_SHARDED = ("w_rec_o", "w_out", "w_ff2")
REPLICATED = (("ln1_g", (1, D)), ("b_in", (1, D_IN)), ("rpb", (N_HEADS * N_RPB_R, N_RPB_C)),
              ("conv_b", (1, D_REC)), ("w_rg_a", (2 * N_REC_BLOCKS * REC_BLOCK, REC_BLOCK)),
              ("w_rg_i", (2 * N_REC_BLOCKS * REC_BLOCK, REC_BLOCK)), ("ln2_g", (1, D)), ("lnf_g", (1, D)))
GATE_BLOCKS = ("w_rg_a", "w_rg_i")
SMALL_ROWS = 112


def _chan_bits(vectors):
    chan = jnp.concatenate(vectors, axis=0)
    bits = lax.bitcast_convert_type(chan, BF16).reshape(-1)
    return jnp.pad(bits, (0, CHAN_BLOCK_ROWS * D - bits.shape[0])).reshape(CHAN_BLOCK_ROWS, D)


def _chan_from_bits(gathered):
    bits = gathered.reshape(N_DEV, CHAN_BLOCK_ROWS * D)[:, :2 * N_CHAN_ROWS * LANES]
    chan = lax.bitcast_convert_type(bits.reshape(N_DEV, N_CHAN_ROWS, LANES, 2), F32)
    return chan.transpose(1, 0, 2).reshape(N_CHAN_ROWS, D)


def kernel(x, ln1_g, w_in, b_in, rpb, w_att_o, conv_w, conv_b, w_rg_a, b_rg_a, w_rg_i, b_rg_i, lru_lambda, w_rec_o, w_out, ln2_g, w_ff1, w_ff2, lnf_g, loss_target, m_ln1_g, m_w_in, m_b_in, m_rpb, m_w_att_o, m_conv_w, m_conv_b, m_w_rg_a, m_b_rg_a, m_w_rg_i, m_b_rg_i, m_lru_lambda, m_w_rec_o, m_w_out, m_ln2_g, m_w_ff1, m_w_ff2, m_lnf_g, v_ln1_g, v_w_in, v_b_in, v_rpb, v_w_att_o, v_conv_w, v_conv_b, v_w_rg_a, v_b_rg_a, v_w_rg_i, v_b_rg_i, v_lru_lambda, v_w_rec_o, v_w_out, v_ln2_g, v_w_ff1, v_w_ff2, v_lnf_g):
    w = dict(zip(NAMES, (ln1_g, w_in, b_in, rpb, w_att_o, conv_w, conv_b, w_rg_a, b_rg_a, w_rg_i,
                         b_rg_i, lru_lambda, w_rec_o, w_out, ln2_g, w_ff1, w_ff2, lnf_g)))
    m = dict(zip(NAMES, (m_ln1_g, m_w_in, m_b_in, m_rpb, m_w_att_o, m_conv_w, m_conv_b, m_w_rg_a,
                         m_b_rg_a, m_w_rg_i, m_b_rg_i, m_lru_lambda, m_w_rec_o, m_w_out, m_ln2_g,
                         m_w_ff1, m_w_ff2, m_lnf_g)))
    v = dict(zip(NAMES, (v_ln1_g, v_w_in, v_b_in, v_rpb, v_w_att_o, v_conv_w, v_conv_b, v_w_rg_a,
                         v_b_rg_a, v_w_rg_i, v_b_rg_i, v_lru_lambda, v_w_rec_o, v_w_out, v_ln2_g,
                         v_w_ff1, v_w_ff2, v_lnf_g)))
    xi, yi, ci = _position()

    shard = {t: w[n][0].T.astype(BF16) for n, t in TRANSPOSED.items()}
    shard.update({n: w[n][0].astype(BF16) for n in ROW_SHARDED})
    shard["chan"] = _chan_bits([w[n][0] for n, _ in CHAN])
    first, later = ("w_in_t", "chan"), ("w_rec_o", "w_out", "w_ff1_t", "w_ff2", "w_att_o_t")
    *gathered, done = _all_gather([shard[n] for n in first], "weight_all_gather")
    p = dict(zip(first, gathered))
    send_sems, recv_sems, sent, zones, token = _gather_start([shard[n] for n in later], done,
                                                             "weight_gather_start")

    def late_weights(after):
        landed = _gather_wait(send_sems, recv_sems, sent, zones, after, "weight_gather_wait")
        return dict(zip(later, _gather_pass_on([shard[n].shape[0] for n in later], landed,
                                               "weight_gather_pass_on")))

    chan = _chan_from_bits(p.pop("chan"))
    r0 = 0
    for n, rows in CHAN:
        p[n] = chan[r0:r0 + rows]
        r0 += rows
    p.update(ln1_g=w["ln1_g"], b_in=w["b_in"] + token[0, 0], rpb=w["rpb"][0], conv_b=w["conv_b"],
             w_rg_a=w["w_rg_a"][0], w_rg_i=w["w_rg_i"][0], ln2_g=w["ln2_g"],
             lnf_g=w["lnf_g"].reshape(1, D))

    core = jnp.reshape(ci, (1,)).astype(jnp.int32)
    chip = jnp.reshape(2 * xi + yi, (1,)).astype(jnp.int32)
    early_sections, late_sections = SECTIONS[1:], SECTIONS[:1]
    in_flight = {}

    def reduce_early(grads):
        chan_g = jnp.concatenate([grads[n] for n, _ in CHAN], axis=0)
        chan_g = chan_g.reshape(N_CHAN_ROWS, N_DEV, LANES).transpose(1, 0, 2).astype(BF16)
        chan_g = jnp.pad(chan_g.reshape(N_DEV, -1), ((0, 0), (0, CHAN_BLOCK_ROWS * D - N_CHAN_ROWS * LANES)))
        grads["chan"] = chan_g.reshape(N_DEV * CHAN_BLOCK_ROWS, D)
        grads["gates"] = jnp.concatenate([grads[n].reshape(-1, D) for n in GATE_BLOCKS], axis=0).astype(BF16)
        sect = [grads[n] for n, _, _ in early_sections]
        got = _pair_exchange(early_sections, sect, "grad_pair_exchange_early")
        parts = _pair_add(early_sections, sect, got, core, "grad_pair_add_early")
        in_flight["early"] = _chip_exchange_start(early_sections, parts, "grad_chip_exchange_start")
        return in_flight["early"][-1][0, 0]

    loss_part, grad_x, grads = _local_step(x[0], loss_target[0], p, late_weights, reduce_early)
    sect = [grads[n] for n, _, _ in late_sections]
    got = _pair_exchange(late_sections, sect, "grad_pair_exchange_late")
    late_parts = _pair_add(late_sections, sect, got, core, "grad_pair_add_late")
    in_flight["late"] = _chip_exchange_start(late_sections, late_parts, "grad_chip_exchange_start_late")

    def finish(group, sections, after, name):
        send_sems, recv_sems, parts, zones, _ = in_flight[group]
        parts, far = _chip_exchange_wait(sections, send_sems, recv_sems, parts, zones, after,
                                         "grad_chip_exchange_wait_" + name)
        return dict(zip((n for n, _, _ in sections),
                        _grad_finish(sections, parts, far, chip, "grad_finish_" + name)))

    started_late = in_flight["late"][-1]
    summed = finish("early", early_sections, started_late, "early")

    flat = jnp.concatenate([grads[n].reshape(-1) for n, _ in REPLICATED if n not in GATE_BLOCKS]
                           + [loss_part.reshape(-1) + started_late[0, 0]])
    n_small = flat.shape[0]
    flat = jnp.pad(flat, (0, SMALL_ROWS * LANES - n_small)).reshape(SMALL_ROWS, LANES)
    small_parts, gate_sum, _ = _all_gather([flat, summed["gates"]], "small_grad_all_gather")
    small = _sum_devices(small_parts, SMALL_ROWS, "small_grad_sum").reshape(-1)
    loss = small[n_small - 1]

    g, delta, new_m, new_v = {}, {}, {}, {}

    def update(n, g2, shape2):
        d2, m2, v2 = _adamw(w[n].reshape(shape2), g2, m[n].reshape(shape2), v[n].reshape(shape2),
                            "adamw_" + n)
        g[n], delta[n], new_m[n], new_v[n] = (a.reshape(w[n].shape) for a in (g2, d2, m2, v2))

    o = 0
    for n, shape2 in REPLICATED:
        if n in GATE_BLOCKS:
            k, rows = GATE_BLOCKS.index(n), gate_sum.shape[0] // len(GATE_BLOCKS)
            update(n, gate_sum[k * rows:(k + 1) * rows].reshape(shape2), shape2)
        else:
            size = shape2[0] * shape2[1]
            update(n, small[o:o + size].reshape(shape2), shape2)
            o += size

    for n in ROW_SHARDED:
        update(n, summed[n], summed[n].shape)
    for n, t in TRANSPOSED.items():
        if t in summed:
            update(n, summed[t].T, summed[t].shape[::-1])
    chan_back = summed["chan"].reshape(-1)[:N_CHAN_ROWS * LANES].reshape(N_CHAN_ROWS, LANES)
    r0 = 0
    for n, rows in CHAN:
        update(n, chan_back[r0:r0 + rows], (rows, LANES))
        r0 += rows
    summed = finish("late", late_sections, delta["lru_lambda"], "late")
    update("w_in", summed["w_in_t"].T, summed["w_in_t"].shape[::-1])

    return (loss, grad_x[None], *[g[n] for n in NAMES], *[delta[n] for n in NAMES],
            *[new_m[n] for n in NAMES], *[new_v[n] for n in NAMES])
```

```python
import math

import numpy as np
import jax
import jax.numpy as jnp
from jax import lax
from jax.experimental import pallas as pl
from jax.experimental.pallas import tpu as pltpu

F32 = jnp.float32
BF16 = jnp.bfloat16

T = 2048
D = 1024
D_ATT = 512
D_REC = 1024
D_FF = 4096
D_IN = 5632
N_HEADS = 8
DH = 64
GRID_W = 64
ROWS = T // GRID_W
WIN_H = 8
WIN_W = 16
KWIN = WIN_H * GRID_W
N_RPB_R = 2 * WIN_H - 1
N_RPB_C = 2 * WIN_W - 1
N_REC_BLOCKS = 16
REC_BLOCK = 64
CG = 128
N_CG = D_REC // CG
LRU_C = 8.0
EPS = 1e-6
N_DEV = 8
N_CHIPS = 4
LANES = 128

ADAM_LR = 0.001
ADAM_B1 = 0.9
ADAM_B2 = 0.999
ADAM_EPS = 1e-08
ADAM_WD = 0.01
ADAM_STEP = 10

MESH_AXES = ("x", "y", "c")
VMEM_LIMIT = 56 * 1024 * 1024

TILE = 512
DZ_ARRAYS = ((0, 3, 1), (3, 4, 2), (7, 4, 2))
N_DZ_TILES = D_IN // TILE


def _params(**kw):
    return pltpu.CompilerParams(vmem_limit_bytes=VMEM_LIMIT, **kw)


def _att_tables():
    rq = np.arange(2 * GRID_W) % GRID_W
    kc = np.arange(KWIN) % GRID_W
    win_start = np.clip(rq - WIN_W // 2, 0, GRID_W - WIN_W)
    valid = (kc[None, :] >= win_start[:, None]) & (kc[None, :] < win_start[:, None] + WIN_W)
    return valid.astype(np.float32), _pair_mask()


def _pair_mask():
    half = np.arange(2 * DH) // DH
    return (half[:, None] == half[None, :]).astype(np.float32)


def _dup_table():
    return np.concatenate([np.eye(REC_BLOCK, dtype=np.float32)] * 2, axis=1)


def _sigmoid(x):
    return 0.5 * jnp.tanh(0.5 * x) + 0.5


def _softplus(x):
    return jnp.maximum(x, 0.0) + jnp.log(1.0 + jnp.exp(-jnp.abs(x)))


def _one_minus_square(log_a, a):
    x = 2.0 * log_a
    series = -x * (1.0 + x * (0.5 + x * (1.0 / 6.0)))
    return jnp.where(x > -0.02, series, 1.0 - a * a)


_GELU_C = math.sqrt(2.0 / math.pi)


def _gelu_and_grad(x):
    x2 = x * x
    inner = _GELU_C * (x + 0.044715 * x * x2)
    t = jnp.tanh(inner)
    g = 0.5 * x * (1.0 + t)
    dg = 0.5 * (1.0 + t) + 0.5 * x * (1.0 - t * t) * _GELU_C * (1.0 + 3.0 * 0.044715 * x2)
    return g, dg


def _dot(a, b):
    return jnp.dot(a, b, preferred_element_type=F32)


def _dot_nt(a, b):
    return lax.dot_general(a, b, (((1,), (1,)), ((), ())), preferred_element_type=F32)


def _dot_tn(a, b):
    return lax.dot_general(a, b, (((0,), (0,)), ((), ())), preferred_element_type=F32)


def _dot_exact(a, b):
    return jnp.dot(a, b, precision=lax.Precision.HIGHEST, preferred_element_type=F32)


def _shift_rows(x, s):
    n = x.shape[0]
    rows = lax.broadcasted_iota(jnp.int32, x.shape, 0)
    y = pltpu.roll(x, s % n, 0)
    if s > 0:
        return jnp.where(rows >= s, y, 0.0)
    return jnp.where(rows < n + s, y, 0.0)


def _rms_bwd(dh, xh, r, g):
    dxh = dh * g
    return r * (dxh - xh * jnp.mean(dxh * xh, axis=-1, keepdims=True))


def _matmul(a, b, mode, out_dtype, name, tm=512, tn=1024, tk=2048):
    if mode == "nn":
        (m, k), (k2, n) = a.shape, b.shape
    elif mode == "nt":
        (m, k), (n, k2) = a.shape, b.shape
    else:
        (k, m), (k2, n) = a.shape, b.shape
    assert k == k2
    tm, tn, tk = min(tm, m), min(tn, n), min(tk, k)
    assert m % tm == 0 and n % tn == 0 and k % tk == 0
    nk = k // tk
    dot = {"nn": _dot, "nt": _dot_nt, "tn": _dot_tn}[mode]

    def body(a_ref, b_ref, o_ref, acc):
        kk = pl.program_id(2)
        part = dot(a_ref[...].astype(BF16), b_ref[...].astype(BF16))
        if nk == 1:
            o_ref[...] = part.astype(out_dtype)
            return

        @pl.when(kk == 0)
        def _():
            acc[...] = part

        @pl.when(kk > 0)
        def _():
            acc[...] += part

        @pl.when(kk == nk - 1)
        def _():
            o_ref[...] = acc[...].astype(out_dtype)

    if mode == "tn":
        a_spec = pl.BlockSpec((tk, tm), lambda i, j, kk: (kk, i))
    else:
        a_spec = pl.BlockSpec((tm, tk), lambda i, j, kk: (i, kk))
    if mode == "nt":
        b_spec = pl.BlockSpec((tn, tk), lambda i, j, kk: (j, kk))
    else:
        b_spec = pl.BlockSpec((tk, tn), lambda i, j, kk: (kk, j))
    return pl.pallas_call(
        body, name=name,
        out_shape=jax.ShapeDtypeStruct((m, n), out_dtype),
        grid=(m // tm, n // tn, nk),
        in_specs=[a_spec, b_spec],
        out_specs=pl.BlockSpec((tm, tn), lambda i, j, kk: (i, j)),
        scratch_shapes=[pltpu.VMEM((tm, tn) if nk > 1 else (8, LANES), F32)],
        compiler_params=_params(dimension_semantics=("parallel", "parallel", "arbitrary")),
    )(a, b)


def _in_proj(x, g1, w_in_t, b_in):
    tm = 1024

    def body(x_ref, g_ref, w_ref, b_ref, qkv_ref, uy_ref, gg_ref, h_ref, h_scr):
        j = pl.program_id(1)

        @pl.when(j == 0)
        def _():
            xv = x_ref[...]
            r = lax.rsqrt(jnp.mean(xv * xv, axis=-1, keepdims=True) + EPS)
            h = ((xv * r) * g_ref[...]).astype(BF16)
            h_scr[...] = h
            h_ref[...] = h

        z = _dot_nt(h_scr[...], w_ref[...]) + b_ref[...]

        @pl.when(j < 3)
        def _():
            qkv_ref[...] = z.astype(BF16)

        @pl.when((j >= 3) & (j < 7))
        def _():
            uy_ref[...] = z

        @pl.when(j >= 7)
        def _():
            gg_ref[...] = z

    return pl.pallas_call(
        body, name="in_proj",
        out_shape=(jax.ShapeDtypeStruct((T, 3 * D_ATT), BF16),
                   jax.ShapeDtypeStruct((T, 2 * D_REC), F32),
                   jax.ShapeDtypeStruct((T, 2 * D), F32),
                   jax.ShapeDtypeStruct((T, D), BF16)),
        grid=(T // tm, N_DZ_TILES),
        in_specs=[pl.BlockSpec((tm, D), lambda i, j: (i, 0)),
                  pl.BlockSpec((1, D), lambda i, j: (0, 0)),
                  pl.BlockSpec((TILE, D), lambda i, j: (j, 0)),
                  pl.BlockSpec((1, TILE), lambda i, j: (0, j))],
        out_specs=(pl.BlockSpec((tm, TILE), lambda i, j: (i, jnp.minimum(j, 2))),
                   pl.BlockSpec((tm, TILE), lambda i, j: (i, jnp.clip(j - 3, 0, 3))),
                   pl.BlockSpec((tm, TILE), lambda i, j: (i, jnp.clip(j - 7, 0, 3))),
                   pl.BlockSpec((tm, D), lambda i, j: (i, 0))),
        scratch_shapes=[pltpu.VMEM((tm, D), BF16)],
        compiler_params=_params(dimension_semantics=("parallel", "arbitrary")),
    )(x, g1, w_in_t, b_in)


def _dz_specs(rows, tile_of, row_of):
    def spec(off, n, per_plane):
        def index(*ids):
            t = jnp.clip(tile_of(*ids) - off, 0, n - 1)
            return (t // per_plane, row_of(*ids), t % per_plane)
        return pl.BlockSpec((1, rows, TILE), index)
    return [spec(off, n, per) for off, n, per in DZ_ARRAYS]


def _dh_norm1_bwd(dz, w_in_t, x, g1, dx1):
    tm = 1024

    def body(*refs):
        seg_refs = refs[:3]
        w_ref, x_ref, g_ref, dx1_ref, gx_ref, dg_ref, acc = refs[3:]
        i, kk = pl.program_id(0), pl.program_id(1)

        @pl.when(kk == 0)
        def _():
            acc[...] = jnp.zeros_like(acc)

        for s, (off, n, _) in enumerate(DZ_ARRAYS):
            @pl.when((kk >= off) & (kk < off + n))
            def _(s=s):
                acc[...] += _dot(seg_refs[s][0], w_ref[...])

        @pl.when((i == 0) & (kk == 0))
        def _():
            dg_ref[...] = jnp.zeros_like(dg_ref)

        @pl.when(kk == N_DZ_TILES - 1)
        def _():
            xv = x_ref[...]
            r = lax.rsqrt(jnp.mean(xv * xv, axis=-1, keepdims=True) + EPS)
            xh = xv * r
            dh = acc[...]
            dg_ref[...] += jnp.sum(dh * xh, axis=0, keepdims=True)
            gx_ref[...] = dx1_ref[...] + _rms_bwd(dh, xh, r, g_ref[...])

    tok = pl.BlockSpec((tm, D), lambda i, j: (i, 0))
    vec = pl.BlockSpec((1, D), lambda i, j: (0, 0))
    return pl.pallas_call(
        body, name="dh_norm1_bwd",
        out_shape=(jax.ShapeDtypeStruct((T, D), F32), jax.ShapeDtypeStruct((1, D), F32)),
        grid=(T // tm, N_DZ_TILES),
        in_specs=_dz_specs(tm, lambda i, j: j, lambda i, j: i)
        + [pl.BlockSpec((TILE, D), lambda i, j: (j, 0)), tok, vec, tok],
        out_specs=(tok, vec),
        scratch_shapes=[pltpu.VMEM((tm, D), F32)],
        compiler_params=_params(dimension_semantics=("arbitrary", "arbitrary")),
    )(*dz, w_in_t, x, g1, dx1)


def _grad_w_in(dz, h):
    def body(*refs):
        seg_refs = refs[:3]
        h_ref, gw_ref, gb_ref = refs[3:]
        j = pl.program_id(0)

        for s, (off, n, _) in enumerate(DZ_ARRAYS):
            @pl.when((j >= off) & (j < off + n))
            def _(s=s):
                a = seg_refs[s][0]
                gw_ref[...] = _dot_tn(a, h_ref[...]).astype(BF16)
                gb_ref[...] = jnp.sum(a.astype(F32), axis=0, keepdims=True)

    return pl.pallas_call(
        body, name="grad_w_in",
        out_shape=(jax.ShapeDtypeStruct((D_IN, D), BF16), jax.ShapeDtypeStruct((1, D_IN), F32)),
        grid=(N_DZ_TILES,),
        in_specs=_dz_specs(T, lambda j: j, lambda j: 0) + [pl.BlockSpec((T, D), lambda j: (0, 0))],
        out_specs=(pl.BlockSpec((TILE, D), lambda j: (j, 0)), pl.BlockSpec((1, TILE), lambda j: (0, j))),
        compiler_params=_params(dimension_semantics=("parallel",)),
    )(*dz, h)


def _rpb_rows(rpb):
    padded = jnp.pad(rpb, ((0, 0), (0, 0), (0, GRID_W - N_RPB_C)))
    rows = [padded[:, WIN_H - 1 - oi: 2 * WIN_H - 1 - oi].reshape(N_HEADS // 2, 2, KWIN)
            for oi in range(WIN_H)]
    return jnp.stack(rows, axis=0)


SKEW = KWIN - (WIN_W - 1)


MASKED = -1e30


def _bias_tiles(rows_ref, valid, bias_s):
    for oi in range(WIN_H):
        for hh in range(2):
            row = jnp.broadcast_to(rows_ref[oi, 0, hh:hh + 1, :], (GRID_W, KWIN))
            tile = pltpu.roll(row, SKEW, 1, stride=1, stride_axis=0)
            bias_s[oi, hh * GRID_W:(hh + 1) * GRID_W, :] = jnp.where(valid[:GRID_W], tile, MASKED)


def _bias_tile_grads(gb_s, flip, out_ref):
    for oi in range(WIN_H):
        for hh in range(2):
            g = _dot_exact(flip, gb_s[oi, hh * GRID_W:(hh + 1) * GRID_W, :])
            back = pltpu.roll(g, KWIN - (GRID_W - WIN_W), 1, stride=1, stride_axis=0)
            out_ref[0, oi, hh:hh + 1, :] = jnp.sum(back, axis=0, keepdims=True)


def _rpb_fold(row_grads):
    g = row_grads.transpose(1, 0, 2, 3).reshape(WIN_H, N_HEADS, WIN_H, GRID_W)
    g = g.transpose(0, 2, 1, 3)

    def body(g_ref, o_ref):
        for dr in range(N_RPB_R):
            terms = [g_ref[oi, i] for oi in range(WIN_H) for i in range(WIN_H) if i - oi + WIN_H - 1 == dr]
            acc = terms[0]
            for term in terms[1:]:
                acc = acc + term
            o_ref[dr] = acc

    out = pl.pallas_call(
        body, name="rpb_fold",
        out_shape=jax.ShapeDtypeStruct((N_RPB_R, N_HEADS, GRID_W), F32),
    )(g)
    return out.transpose(1, 0, 2)[:, :, :N_RPB_C]


def _att_scores(q_ref, k_ref, bias_ref, hmask, r):
    rs = jnp.clip(r - WIN_H // 2, 0, ROWS - WIN_H)
    oi = r - rs
    q0 = pl.multiple_of(r * GRID_W, GRID_W)
    k0 = pl.multiple_of(rs * GRID_W, GRID_W)
    q_r = q_ref[pl.ds(q0, GRID_W), :] * (DH ** -0.5)
    q2 = jnp.where(hmask, jnp.concatenate([q_r, q_r], axis=0), jnp.zeros((), BF16))
    kw = k_ref[pl.ds(k0, KWIN), :]
    s = _dot_nt(q2, kw) + bias_ref[oi]
    e = jnp.exp(s - jnp.max(s, axis=-1, keepdims=True))
    return e, 1.0 / jnp.sum(e, axis=-1, keepdims=True), q2, kw, q0, k0, oi


def _att_fwd(qkv, bias_rows):
    valid_np, hmask_np = _att_tables()

    def body(q_ref, k_ref, v_ref, rows_ref, valid_ref, hmask_ref, o_ref, bias_s):
        valid = valid_ref[...] > 0.5
        hmask = hmask_ref[...] > 0.5
        first_head = lax.broadcasted_iota(jnp.int32, (GRID_W, 2 * DH), 1) < DH
        _bias_tiles(rows_ref, valid, bias_s)

        def row(r, carry):
            e, rl, _, _, q0, k0, _ = _att_scores(q_ref, k_ref, bias_s, hmask, r)
            o2 = _dot((e * rl).astype(BF16), v_ref[pl.ds(k0, KWIN), :])
            o_ref[pl.ds(q0, GRID_W), :] = jnp.where(first_head, o2[:GRID_W], o2[GRID_W:]).astype(BF16)
            return carry

        lax.fori_loop(0, ROWS, row, 0, unroll=4)

    col = lambda off: pl.BlockSpec((T, 2 * DH), lambda hp: (0, hp + off))
    return pl.pallas_call(
        body, name="att_fwd",
        out_shape=jax.ShapeDtypeStruct((T, D_ATT), BF16),
        grid=(N_HEADS // 2,),
        in_specs=[col(0), col(4), col(8),
                  pl.BlockSpec((WIN_H, 1, 2, KWIN), lambda hp: (0, hp, 0, 0)),
                  pl.BlockSpec((2 * GRID_W, KWIN), lambda hp: (0, 0)),
                  pl.BlockSpec((2 * DH, 2 * DH), lambda hp: (0, 0))],
        out_specs=pl.BlockSpec((T, 2 * DH), lambda hp: (0, hp)),
        scratch_shapes=[pltpu.VMEM((WIN_H, 2 * GRID_W, KWIN), F32)],
        compiler_params=_params(dimension_semantics=("parallel",)),
    )(qkv, qkv, qkv, bias_rows, jnp.asarray(valid_np), jnp.asarray(hmask_np))


def _att_bwd(qkv, bias_rows, datt, after):
    valid_np, hmask_np = _att_tables()

    def body(q_ref, k_ref, v_ref, do_ref, rows_ref, valid_ref, hmask_ref, flip_ref,
             dqkv_ref, grows_ref, dk_acc, dv_acc, bias_s, gb_s):
        valid = valid_ref[...] > 0.5
        hmask = hmask_ref[...] > 0.5
        first_head = lax.broadcasted_iota(jnp.int32, (GRID_W, 2 * DH), 1) < DH
        dk_acc[...] = jnp.zeros_like(dk_acc)
        dv_acc[...] = jnp.zeros_like(dv_acc)
        gb_s[...] = jnp.zeros_like(gb_s)
        _bias_tiles(rows_ref, valid, bias_s)

        def row(r, carry):
            e, rl, q2, kw, q0, k0, oi = _att_scores(q_ref, k_ref, bias_s, hmask, r)
            do_r = do_ref[pl.ds(q0, GRID_W), :]
            do2 = jnp.where(hmask, jnp.concatenate([do_r, do_r], axis=0), jnp.zeros((), BF16))
            vw = v_ref[pl.ds(k0, KWIN), :]
            p = e * rl
            dp = _dot_nt(do2, vw)
            ds = p * (dp - jnp.sum(dp * p, axis=-1, keepdims=True))
            p16 = p.astype(BF16)
            ds16 = ds.astype(BF16)
            dv_acc[pl.ds(k0, KWIN), :] += _dot_tn(p16, do2)
            dk_acc[pl.ds(k0, KWIN), :] += _dot_tn(ds16, q2)
            dq2 = _dot(ds16, kw) * (DH ** -0.5)
            dqkv_ref[0, pl.ds(q0, GRID_W), :] = jnp.where(first_head, dq2[:GRID_W], dq2[GRID_W:]).astype(BF16)
            gb_s[oi] += ds
            return carry

        lax.fori_loop(0, ROWS, row, 0, unroll=4)
        dqkv_ref[1] = dk_acc[...].astype(BF16)
        dqkv_ref[2] = dv_acc[...].astype(BF16)
        _bias_tile_grads(gb_s, flip_ref[...], grows_ref)

    col = lambda off: pl.BlockSpec((T, 2 * DH), lambda hp: (0, hp + off))
    tiles = pltpu.VMEM((WIN_H, 2 * GRID_W, KWIN), F32)
    return pl.pallas_call(
        body, name="att_bwd",
        out_shape=(jax.ShapeDtypeStruct((3, T, D_ATT), BF16),
                   jax.ShapeDtypeStruct((N_HEADS // 2, WIN_H, 2, KWIN), F32)),
        grid=(N_HEADS // 2,),
        in_specs=[col(0), col(4), col(8), col(0),
                  pl.BlockSpec((WIN_H, 1, 2, KWIN), lambda hp: (0, hp, 0, 0)),
                  pl.BlockSpec((2 * GRID_W, KWIN), lambda hp: (0, 0)),
                  pl.BlockSpec((2 * DH, 2 * DH), lambda hp: (0, 0)),
                  pl.BlockSpec((GRID_W, GRID_W), lambda hp: (0, 0))],
        out_specs=(pl.BlockSpec((3, T, 2 * DH), lambda hp: (0, 0, hp)),
                   pl.BlockSpec((1, WIN_H, 2, KWIN), lambda hp: (hp, 0, 0, 0))),
        scratch_shapes=[pltpu.VMEM((T, 2 * DH), F32), pltpu.VMEM((T, 2 * DH), F32), tiles, tiles],
        compiler_params=_params(dimension_semantics=("parallel",)),
    )(qkv, qkv, qkv, datt, bias_rows, jnp.asarray(valid_np) + after, jnp.asarray(hmask_np),
      jnp.asarray(np.eye(GRID_W, dtype=np.float32)[::-1].copy()))


def _conv_taps(up):
    return (_shift_rows(up, 2), _shift_rows(up, 1), up, _shift_rows(up, -1))


def _pair_block_diag(w_pair, dup, same_half):
    return jnp.where(same_half, _dot(w_pair.astype(BF16), dup), 0.0).astype(BF16)


def _gates(u, u16, wa, ba, wi, bi, lam):
    r = _sigmoid(_dot(u16, wa) + ba)
    ig = _sigmoid(_dot(u16, wi) + bi)
    sp = _softplus(-lam)
    log_a = (-LRU_C) * r * sp
    a = jnp.exp(log_a)
    mult2 = jnp.maximum(_one_minus_square(log_a, a), 0.0)
    return r, ig, sp, a, jnp.sqrt(mult2), mult2


SCAN_BLOCKS = 2


def _scans(jobs):
    c = jobs[0][0].shape[1]
    nblk = T // 8
    rows = lax.broadcasted_iota(jnp.int32, (8, c), 0)

    def block(a, b, reverse):
        for s in (1, 2, 4):
            if reverse:
                keep = rows < 8 - s
                a_s = jnp.where(keep, pltpu.roll(a, 8 - s, 0), 1.0)
                b_s = jnp.where(keep, pltpu.roll(b, 8 - s, 0), 0.0)
            else:
                keep = rows >= s
                a_s = jnp.where(keep, pltpu.roll(a, s, 0), 1.0)
                b_s = jnp.where(keep, pltpu.roll(b, s, 0), 0.0)
            b = a * b_s + b
            a = a * a_s
        return a, b

    def step(i, carry):
        out = []
        for (a_ref, b_ref, h_ref, reverse), h_prev in zip(jobs, carry):
            for u in range(SCAN_BLOCKS):
                blk = i * SCAN_BLOCKS + u
                if reverse:
                    blk = nblk - 1 - blk
                t0 = pl.multiple_of(blk * 8, 8)
                a, b = block(a_ref[pl.ds(t0, 8), :], b_ref[pl.ds(t0, 8), :], reverse)
                h = a * h_prev + b
                h_ref[pl.ds(t0, 8), :] = h
                h_prev = jnp.broadcast_to(h[0:1] if reverse else h[7:8], (8, c))
            out.append(h_prev)
        return tuple(out)

    lax.fori_loop(0, nblk // SCAN_BLOCKS, step, tuple(jnp.zeros((8, c), F32) for _ in jobs))


def _rec_specs():
    tok = lambda off: pl.BlockSpec((T, CG), lambda g: (0, g + off))
    per_ch = lambda rows: pl.BlockSpec((rows, CG), lambda g: (0, g))
    wspec = pl.BlockSpec((2, 1, CG, REC_BLOCK), lambda g: (0, g, 0, 0))
    const = lambda shape: pl.BlockSpec(shape, lambda g: (0, 0))
    return tok, per_ch, wspec, const


def _rec_fwd(uy, conv_w, conv_b, w_a, b_a, w_i, b_i, lam):
    tok, per_ch, wspec, const = _rec_specs()

    def body(up_ref, yb_ref, cw_ref, cb_ref, wa_ref, ba_ref, wi_ref, bi_ref, lam_ref, dup_ref, half_ref,
             hf_ref, hb_ref, yrec_ref, a_f, bx_f, a_b, bx_b):
        dup = dup_ref[...]
        same_half = half_ref[...] > 0.5
        taps = _conv_taps(up_ref[...])
        u = cb_ref[...]
        for j in range(4):
            u = u + taps[j] * cw_ref[j:j + 1, :]
        u16 = u.astype(BF16)
        for d, (a_s, bx_s) in enumerate(((a_f, bx_f), (a_b, bx_b))):
            wa = _pair_block_diag(wa_ref[d, 0], dup, same_half)
            wi = _pair_block_diag(wi_ref[d, 0], dup, same_half)
            _, ig, _, a, mult, _ = _gates(u, u16, wa, ba_ref[d:d + 1, :], wi, bi_ref[d:d + 1, :],
                                       lam_ref[d:d + 1, :])
            a_s[...] = a
            bx_s[...] = mult * (ig * u)
        _scans([(a_f, bx_f, hf_ref, False), (a_b, bx_b, hb_ref, True)])
        gelu, _ = _gelu_and_grad(yb_ref[...])
        yrec_ref[...] = ((hf_ref[...] + hb_ref[...]) * gelu).astype(BF16)

    return pl.pallas_call(
        body, name="rec_fwd",
        out_shape=(jax.ShapeDtypeStruct((T, D_REC), F32), jax.ShapeDtypeStruct((T, D_REC), F32),
                   jax.ShapeDtypeStruct((T, D_REC), BF16)),
        grid=(N_CG,),
        in_specs=[tok(0), tok(N_CG), per_ch(4), per_ch(1), wspec, per_ch(2), wspec, per_ch(2), per_ch(2),
                  const((REC_BLOCK, CG)), const((CG, CG))],
        out_specs=(tok(0), tok(0), tok(0)),
        scratch_shapes=[pltpu.VMEM((T, CG), F32)] * 4,
        compiler_params=_params(dimension_semantics=("parallel",)),
    )(uy, uy, conv_w, conv_b, w_a, b_a, w_i, b_i, lam,
      jnp.asarray(_dup_table(), BF16), jnp.asarray(_pair_mask()))


def _rec_bwd(uy, hf, hb, dyrec, conv_w, conv_b, w_a, b_a, w_i, b_i, lam):
    tok, per_ch, wspec, const = _rec_specs()

    def body(up_ref, yb_ref, hf_ref, hb_ref, dy_ref, cw_ref, cb_ref, wa_ref, ba_ref, wi_ref, bi_ref,
             lam_ref, dup_ref, dupt_ref, half_ref,
             duy_ref, dcw_ref, dcb_ref, dwa_ref, dba_ref, dwi_ref, dbi_ref, dlam_ref,
             a_s0, a_s1, dh_s, g_s0, g_s1):
        dup = dup_ref[...]
        dup_t = dupt_ref[...]
        same_half = half_ref[...] > 0.5
        taps = _conv_taps(up_ref[...])
        u = cb_ref[...]
        for j in range(4):
            u = u + taps[j] * cw_ref[j:j + 1, :]
        u16 = u.astype(BF16)
        gelu, dgelu = _gelu_and_grad(yb_ref[...])
        dy = dy_ref[...]
        duy_ref[1] = (dy * (hf_ref[...] + hb_ref[...]) * dgelu).astype(BF16)
        dh_s[...] = dy * gelu
        gate_values = []
        for d, a_s in enumerate((a_s0, a_s1)):
            wa = _pair_block_diag(wa_ref[d, 0], dup, same_half)
            wi = _pair_block_diag(wi_ref[d, 0], dup, same_half)
            lam_d = lam_ref[d:d + 1, :]
            r, ig, sp, a, mult, mult2 = _gates(u, u16, wa, ba_ref[d:d + 1, :], wi, bi_ref[d:d + 1, :], lam_d)
            a_s[...] = _shift_rows(a, 1 if d == 1 else -1)
            gate_values.append((wa, wi, lam_d, r, ig, sp, a, mult, mult2))
        _scans([(a_s0, dh_s, g_s0, True), (a_s1, dh_s, g_s1, False)])
        du = jnp.zeros((T, CG), F32)
        for d, g_s in enumerate((g_s0, g_s1)):
            reverse = d == 1
            wa, wi, lam_d, r, ig, sp, a, mult, mult2 = gate_values[d]
            g = g_s[...]
            h_prev = _shift_rows(hb_ref[...], -1) if reverse else _shift_rows(hf_ref[...], 1)
            da = g * h_prev
            dmult = g * (ig * u)
            dig = g * mult * u
            du = du + g * mult * ig
            dmult_dlog = jnp.where(mult2 > 0.0, -(a * a) * lax.rsqrt(mult2), 0.0)
            dlog_a = da * a + dmult * dmult_dlog
            dr = dlog_a * ((-LRU_C) * sp)
            dsp = jnp.sum(dlog_a * ((-LRU_C) * r), axis=0, keepdims=True)
            dlam_ref[d:d + 1, :] = dsp * (-_sigmoid(-lam_d))
            dga = dr * r * (1.0 - r)
            dgi = dig * ig * (1.0 - ig)
            dga16 = dga.astype(BF16)
            dgi16 = dgi.astype(BF16)
            du = du + _dot_nt(dga16, wa) + _dot_nt(dgi16, wi)
            dwa_ref[d, 0] = _dot_exact(jnp.where(same_half, _dot_tn(u16, dga16), 0.0), dup_t)
            dwi_ref[d, 0] = _dot_exact(jnp.where(same_half, _dot_tn(u16, dgi16), 0.0), dup_t)
            dba_ref[d:d + 1, :] = jnp.sum(dga, axis=0, keepdims=True)
            dbi_ref[d:d + 1, :] = jnp.sum(dgi, axis=0, keepdims=True)
        dcb_ref[...] = jnp.sum(du, axis=0, keepdims=True)
        for j in range(4):
            dcw_ref[j:j + 1, :] = jnp.sum(du * taps[j], axis=0, keepdims=True)
        dup_in = (_shift_rows(du, -2) * cw_ref[0:1, :] + _shift_rows(du, -1) * cw_ref[1:2, :]
                  + du * cw_ref[2:3, :] + _shift_rows(du, 1) * cw_ref[3:4, :])
        duy_ref[0] = dup_in.astype(BF16)

    wshape = jax.ShapeDtypeStruct((2, N_CG, CG, REC_BLOCK), F32)
    vec = lambda rows: jax.ShapeDtypeStruct((rows, D_REC), F32)
    dup_np = _dup_table()
    return pl.pallas_call(
        body, name="rec_bwd",
        out_shape=(jax.ShapeDtypeStruct((2, T, D_REC), BF16),
                   vec(4), vec(1), wshape, vec(2), wshape, vec(2), vec(2)),
        grid=(N_CG,),
        in_specs=[tok(0), tok(N_CG), tok(0), tok(0), tok(0),
                  per_ch(4), per_ch(1), wspec, per_ch(2), wspec, per_ch(2), per_ch(2),
                  const((REC_BLOCK, CG)), const((CG, REC_BLOCK)), const((CG, CG))],
        out_specs=(pl.BlockSpec((2, T, CG), lambda g: (0, 0, g)),
                   per_ch(4), per_ch(1), wspec, per_ch(2), wspec, per_ch(2), per_ch(2)),
        scratch_shapes=[pltpu.VMEM((T, CG), F32)] * 5,
        compiler_params=_params(dimension_semantics=("parallel",)),
    )(uy, uy, hf, hb, dyrec, conv_w, conv_b, w_a, b_a, w_i, b_i, lam,
      jnp.asarray(dup_np, BF16), jnp.asarray(dup_np.T.copy()), jnp.asarray(_pair_mask()))


TM_MIX = 256


def _mix_specs():
    tok = lambda width, blk=0: pl.BlockSpec((TM_MIX, width), lambda i: (i, blk))
    full = lambda shape: pl.BlockSpec(shape, lambda i: (0, 0))
    return tok, full


def _mix_fwd(x, att, yrec, gg, w_att_o_t, w_rec_o, w_out):
    tok, full = _mix_specs()

    def body(x_ref, att_ref, yr_ref, ga_ref, gr_ref, wao_ref, wro_ref, wo_ref, x1_ref, mixed_ref):
        y_att = _dot_nt(att_ref[...], wao_ref[...])
        y_rec = _dot(yr_ref[...], wro_ref[...])
        mixed = (_sigmoid(ga_ref[...]) * y_att + _sigmoid(gr_ref[...]) * y_rec).astype(BF16)
        mixed_ref[...] = mixed
        x1_ref[...] = x_ref[...] + _dot(mixed, wo_ref[...])

    return pl.pallas_call(
        body, name="mix_fwd",
        out_shape=(jax.ShapeDtypeStruct((T, D), F32), jax.ShapeDtypeStruct((T, D), BF16)),
        grid=(T // TM_MIX,),
        in_specs=[tok(D), tok(D_ATT), tok(D_REC), tok(D, 0), tok(D, 1),
                  full((D, D_ATT)), full((D_REC, D)), full((D, D))],
        out_specs=(tok(D), tok(D)),
        compiler_params=_params(dimension_semantics=("parallel",)),
    )(x, att, yrec, gg, gg, w_att_o_t, w_rec_o, w_out)


def _mix_bwd(dx1, att, yrec, gg, w_att_o_t, w_rec_o, w_out):
    tok, full = _mix_specs()

    def body(dx_ref, att_ref, yr_ref, ga_ref, gr_ref, wao_ref, wro_ref, wo_ref,
             dgg_ref, dya_ref, dyr_ref, datt_ref, dyrp_ref):
        dmixed = _dot_nt(dx_ref[...].astype(BF16), wo_ref[...])
        y_att = _dot_nt(att_ref[...], wao_ref[...])
        y_rec = _dot(yr_ref[...], wro_ref[...])
        sa = _sigmoid(ga_ref[...])
        sr = _sigmoid(gr_ref[...])
        dgg_ref[0] = (dmixed * y_att * sa * (1.0 - sa)).astype(BF16)
        dgg_ref[1] = (dmixed * y_rec * sr * (1.0 - sr)).astype(BF16)
        dya = (dmixed * sa).astype(BF16)
        dyr = (dmixed * sr).astype(BF16)
        dya_ref[...] = dya
        dyr_ref[...] = dyr
        datt_ref[...] = _dot(dya, wao_ref[...]).astype(BF16)
        dyrp_ref[...] = _dot_nt(dyr, wro_ref[...])

    return pl.pallas_call(
        body, name="mix_bwd",
        out_shape=(jax.ShapeDtypeStruct((2, T, D), BF16),
                   jax.ShapeDtypeStruct((T, D), BF16), jax.ShapeDtypeStruct((T, D), BF16),
                   jax.ShapeDtypeStruct((T, D_ATT), BF16), jax.ShapeDtypeStruct((T, D_REC), F32)),
        grid=(T // TM_MIX,),
        in_specs=[tok(D), tok(D_ATT), tok(D_REC), tok(D, 0), tok(D, 1),
                  full((D, D_ATT)), full((D_REC, D)), full((D, D))],
        out_specs=(pl.BlockSpec((2, TM_MIX, D), lambda i: (0, i, 0)),
                   tok(D), tok(D), tok(D_ATT), tok(D_REC)),
        compiler_params=_params(dimension_semantics=("parallel",)),
    )(dx1, att, yrec, gg, gg, w_att_o_t, w_rec_o, w_out)


TM_FFN = 256
FF_CHUNK = 1024


def _ffn_loss(x1, target, g2, gf, w_ff1_t, w_ff2):
    n_chunks = D_FF // FF_CHUNK

    def body(x1_ref, tg_ref, g2_ref, gf_ref, w1_hbm, w2_hbm,
             loss_ref, dx1_ref, h2_ref, act_ref, dpre_ref, dx2_ref, dg2_ref, dgf_ref,
             w1, w2, relu_s):
        i = pl.program_id(0)

        @pl.when(i == 0)
        def _():
            pltpu.sync_copy(w1_hbm, w1)
            pltpu.sync_copy(w2_hbm, w2)
            loss_ref[...] = jnp.zeros_like(loss_ref)
            dg2_ref[...] = jnp.zeros_like(dg2_ref)
            dgf_ref[...] = jnp.zeros_like(dgf_ref)

        x1v = x1_ref[...]
        r2 = lax.rsqrt(jnp.mean(x1v * x1v, axis=-1, keepdims=True) + EPS)
        xh2 = x1v * r2
        h2 = (xh2 * g2_ref[...]).astype(BF16)
        h2_ref[...] = h2
        x2 = x1v
        for c in range(n_chunks):
            ff = slice(c * FF_CHUNK, (c + 1) * FF_CHUNK)
            rl = jnp.maximum(_dot_nt(h2, w1[ff, :]), 0.0)
            relu_s[:, ff] = rl
            act = (rl * rl).astype(BF16)
            act_ref[:, ff] = act
            x2 = x2 + _dot(act, w2[ff, :])
        r3 = lax.rsqrt(jnp.mean(x2 * x2, axis=-1, keepdims=True) + EPS)
        xh3 = x2 * r3
        err = xh3 * gf_ref[...] - tg_ref[...]
        loss_ref[...] += 0.5 * jnp.sum(jnp.mean(err * err, axis=-1, keepdims=True))
        dy = err * (1.0 / D)
        dgf_ref[...] += jnp.sum(dy * xh3, axis=0, keepdims=True)
        dx2 = _rms_bwd(dy, xh3, r3, gf_ref[...])
        dx2_16 = dx2.astype(BF16)
        dx2_ref[...] = dx2_16
        dh2 = jnp.zeros((TM_FFN, D), F32)
        for c in range(n_chunks):
            ff = slice(c * FF_CHUNK, (c + 1) * FF_CHUNK)
            dpre = (_dot_nt(dx2_16, w2[ff, :]) * (2.0 * relu_s[:, ff])).astype(BF16)
            dpre_ref[:, ff] = dpre
            dh2 = dh2 + _dot(dpre, w1[ff, :])
        dg2_ref[...] += jnp.sum(dh2 * xh2, axis=0, keepdims=True)
        dx1_ref[...] = dx2 + _rms_bwd(dh2, xh2, r2, g2_ref[...])

    tok = lambda width: pl.BlockSpec((TM_FFN, width), lambda i: (i, 0))
    vec = pl.BlockSpec((1, D), lambda i: (0, 0))
    hbm = pl.BlockSpec(memory_space=pl.ANY)
    return pl.pallas_call(
        body, name="ffn_loss",
        out_shape=(jax.ShapeDtypeStruct((8, 128), F32), jax.ShapeDtypeStruct((T, D), F32),
                   jax.ShapeDtypeStruct((T, D), BF16), jax.ShapeDtypeStruct((T, D_FF), BF16),
                   jax.ShapeDtypeStruct((T, D_FF), BF16), jax.ShapeDtypeStruct((T, D), BF16),
                   jax.ShapeDtypeStruct((1, D), F32), jax.ShapeDtypeStruct((1, D), F32)),
        grid=(T // TM_FFN,),
        in_specs=[tok(D), tok(D), vec, vec, hbm, hbm],
        out_specs=(pl.BlockSpec((8, 128), lambda i: (0, 0)), tok(D), tok(D), tok(D_FF), tok(D_FF), tok(D),
                   vec, vec),
        scratch_shapes=[pltpu.VMEM((D_FF, D), BF16), pltpu.VMEM((D_FF, D), BF16),
                        pltpu.VMEM((TM_FFN, D_FF), F32)],
        compiler_params=_params(dimension_semantics=("arbitrary",)),
    )(x1, target, g2, gf, w_ff1_t, w_ff2)


def _local_step(x, target, p, late_weights, reduce_early):
    bias = _rpb_rows(p["rpb"])
    pairs = lambda w: w.reshape(2, N_CG, CG, REC_BLOCK)
    w_a, w_i = pairs(p["w_rg_a"]), pairs(p["w_rg_i"])
    rec_params = (p["conv_w"], p["conv_b"], w_a, p["b_rg_a"], w_i, p["b_rg_i"], p["lru_lambda"])

    qkv, uy, gg, h = _in_proj(x, p["ln1_g"], p["w_in_t"], p["b_in"])
    att = _att_fwd(qkv, bias)
    hf, hb, yrec = _rec_fwd(uy, *rec_params)
    p = {**p, **late_weights(yrec)}
    x1, mixed = _mix_fwd(x, att, yrec, gg, p["w_att_o_t"], p["w_rec_o"], p["w_out"])
    loss8, dx1, h2, act, dpre, dx2, g_ln2, g_lnf = _ffn_loss(
        x1, target, p["ln2_g"], p["lnf_g"], p["w_ff1_t"], p["w_ff2"])

    dgg, dya, dyr, datt, dyrp = _mix_bwd(dx1, att, yrec, gg, p["w_att_o_t"], p["w_rec_o"], p["w_out"])
    duy, g_cw, g_cb, g_wa, g_ba, g_wi, g_bi, g_lam = _rec_bwd(uy, hf, hb, dyrp, *rec_params)
    blocks = lambda g: g.reshape(2, N_REC_BLOCKS, REC_BLOCK, REC_BLOCK)
    grads = {
        "w_att_o_t": _matmul(dya, att, "tn", BF16, "g_w_att_o"),
        "conv_w": g_cw, "conv_b": g_cb, "w_rg_a": blocks(g_wa), "b_rg_a": g_ba,
        "w_rg_i": blocks(g_wi), "b_rg_i": g_bi, "lru_lambda": g_lam,
        "w_rec_o": _matmul(yrec, dyr, "tn", BF16, "g_w_rec_o"),
        "w_out": _matmul(mixed, dx1, "tn", BF16, "g_w_out"),
        "ln2_g": g_ln2,
        "w_ff1_t": _matmul(dpre, h2, "tn", BF16, "g_w_ff1"),
        "w_ff2": _matmul(act, dx2, "tn", BF16, "g_w_ff2"),
        "lnf_g": g_lnf,
    }
    after = reduce_early(grads)
    dqkv, gbias = _att_bwd(qkv, bias, datt, after)
    dz = (dqkv, duy, dgg)
    grad_x, g_ln1 = _dh_norm1_bwd(dz, p["w_in_t"], x, p["ln1_g"], dx1)
    g_w_in_t, g_b_in = _grad_w_in(dz, h)
    grads.update(ln1_g=g_ln1, w_in_t=g_w_in_t, b_in=g_b_in, rpb=_rpb_fold(gbias))
    return loss8[0:1, 0:1], grad_x, grads


MESH_ID = pl.DeviceIdType.MESH
ANY = pl.BlockSpec(memory_space=pl.ANY)

CHAN_BLOCK_ROWS = 32
GATE_ROWS = 2 * 2 * N_REC_BLOCKS * REC_BLOCK * REC_BLOCK // (N_DEV * D)
SECTIONS = (("w_in_t", 704, D), ("w_rec_o", 128, D), ("w_out", 128, D), ("w_ff1_t", 512, D),
            ("w_ff2", 512, D), ("chan", CHAN_BLOCK_ROWS, D), ("w_att_o_t", 128, D_ATT),
            ("gates", GATE_ROWS, D))
N_SEC = len(SECTIONS)
N_CHAN_ROWS = 10
CHAN = (("conv_w", 4), ("b_rg_a", 2), ("b_rg_i", 2), ("lru_lambda", 2))


def _position():
    return lax.axis_index("x"), lax.axis_index("y"), lax.axis_index("c")


def _other_chips(x, y):
    return [(1 - x, y), (x, 1 - y), (1 - x, 1 - y)]


def _block_of(ref, dev, rows):
    return ref.at[pl.ds(pl.multiple_of(dev * rows, 16), rows)]


def _all_gather(shards, name):
    ns = len(shards)

    def body(*refs):
        x_refs, out_refs, done_ref = refs[:ns], refs[ns:2 * ns], refs[2 * ns]
        send_sems, recv_sems, local_sems = refs[2 * ns + 1:]
        done_ref[0, 0] = 0.0
        x, y, c = _position()
        me, sibling = (x, y, c), (x, y, 1 - c)
        chips = _other_chips(x, y)

        def rows(s, px, py, pc):
            return _block_of(out_refs[s], 4 * px + 2 * py + pc, shards[s].shape[0])

        def copy(k, s, block, to, from_shard=False):
            return pltpu.make_async_remote_copy(
                src_ref=x_refs[s] if from_shard else rows(s, *block), dst_ref=rows(s, *block),
                send_sem=send_sems.at[k * ns + s], recv_sem=recv_sems.at[k * ns + s],
                device_id=to, device_id_type=MESH_ID)

        sections = range(ns)
        mine = [pltpu.make_async_copy(x_refs[s], rows(s, *me), local_sems.at[s]) for s in sections]
        first = [copy(0, s, me, sibling, True) for s in sections]
        first += [copy(1 + j, s, me, (*chip, c), True) for j, chip in enumerate(chips) for s in sections]
        for cp in mine + first:
            cp.start()
        passed = []
        for j, chip in enumerate(chips):
            for s in sections:
                copy(1 + j, s, (*chip, c), me).wait_recv()
                passed.append(copy(4 + j, s, (*chip, c), sibling))
                passed[-1].start()
        for s in sections:
            copy(0, s, sibling, me).wait_recv()
        for j, chip in enumerate(chips):
            for s in sections:
                copy(4 + j, s, (*chip, 1 - c), me).wait_recv()
        for cp in first + passed:
            cp.wait_send()
        for cp in mine:
            cp.wait()

    return pl.pallas_call(
        body, name=name,
        out_shape=tuple(jax.ShapeDtypeStruct((N_DEV * s.shape[0], s.shape[1]), s.dtype) for s in shards)
        + (jax.ShapeDtypeStruct((1, 1), F32),),
        in_specs=[ANY] * ns,
        out_specs=(ANY,) * ns + (pl.BlockSpec(memory_space=pltpu.SMEM),),
        scratch_shapes=[pltpu.SemaphoreType.DMA((7 * ns,)), pltpu.SemaphoreType.DMA((7 * ns,)),
                        pltpu.SemaphoreType.DMA((ns,))],
    )(*shards)


HBM = pl.BlockSpec(memory_space=pltpu.HBM)
SEM = pl.BlockSpec(memory_space=pltpu.SEMAPHORE)
EFFECT = pltpu.SideEffectType.DATAFLOW_SIDE_EFFECTING


def _in_hbm(a):
    return pltpu.with_memory_space_constraint(a, pltpu.HBM)


def _first_hop_copies(shards, x_refs, zones, send_sems, recv_sems):
    ns = len(shards)
    x, y, c = _position()
    targets = [(x, y, 1 - c)] + [(cx, cy, c) for cx, cy in _other_chips(x, y)]
    return [pltpu.make_async_remote_copy(
        src_ref=x_refs[s], dst_ref=_block_of(zones[s], 4 * x + 2 * y + c, shards[s].shape[0]),
        send_sem=send_sems.at[k * ns + s], recv_sem=recv_sems.at[k * ns + s],
        device_id=to, device_id_type=MESH_ID)
        for k, to in enumerate(targets) for s in range(ns)]


def _own_blocks_placed(shards, after):
    ns = len(shards)
    x, y, c = _position()
    me = jnp.reshape(4 * x + 2 * y + c, (1,)).astype(jnp.int32)
    shards = [*shards[:-1], shards[-1] + after.astype(shards[-1].dtype)]

    def body(me_ref, *refs):
        for s in range(ns):
            refs[ns + s][...] = refs[s][...]

    return pl.pallas_call(
        body, name="own_blocks_placed",
        out_shape=tuple(jax.ShapeDtypeStruct((N_DEV * s.shape[0], s.shape[1]), s.dtype) for s in shards),
        grid_spec=pltpu.PrefetchScalarGridSpec(
            num_scalar_prefetch=1, grid=(1,),
            in_specs=[pl.BlockSpec(s.shape, lambda i, me: (0, 0)) for s in shards],
            out_specs=tuple(pl.BlockSpec(s.shape, lambda i, me: (me[0], 0)) for s in shards)),
        compiler_params=_params(dimension_semantics=("arbitrary",)),
    )(me, *shards)


def _gather_start(shards, after, name):
    ns = len(shards)
    zones = _own_blocks_placed(shards, after)

    def body(*refs):
        for cp in _first_hop_copies(shards, refs[:ns], refs[ns:2 * ns], refs[2 * ns], refs[2 * ns + 1]):
            cp.start()
        refs[-1][...] = jnp.zeros_like(refs[-1])

    out = pl.pallas_call(
        body, name=name,
        out_shape=(pltpu.SemaphoreType.DMA((4 * ns,)), pltpu.SemaphoreType.DMA((4 * ns,)),
                   *[pltpu.HBM(a.shape, a.dtype) for a in (*shards, *zones)],
                   jax.ShapeDtypeStruct((8, LANES), F32)),
        in_specs=[HBM] * (2 * ns),
        out_specs=(SEM, SEM, *[HBM] * (2 * ns), pl.BlockSpec(memory_space=pltpu.VMEM)),
        input_output_aliases={i: 2 + i for i in range(2 * ns)},
        compiler_params=pltpu.CompilerParams(has_side_effects=EFFECT),
    )(*[_in_hbm(a) for a in shards], *[_in_hbm(a) for a in zones])
    return out[0], out[1], out[2:2 + ns], out[2 + ns:2 + 2 * ns], out[-1]


def _gather_wait(send_sems, recv_sems, shards, zones, after, name):
    ns = len(shards)

    def body(*refs):
        for cp in _first_hop_copies(shards, refs[:ns], refs[ns:2 * ns], refs[2 * ns], refs[2 * ns + 1]):
            cp.wait_send()
            cp.wait_recv()

    out = pl.pallas_call(
        body, name=name,
        out_shape=tuple(pltpu.HBM(a.shape, a.dtype) for a in (*shards, *zones)),
        in_specs=[HBM] * (2 * ns) + [SEM, SEM, ANY],
        out_specs=(HBM,) * (2 * ns),
        input_output_aliases={i: i for i in range(2 * ns)},
        compiler_params=pltpu.CompilerParams(has_side_effects=EFFECT),
    )(*shards, *zones, send_sems, recv_sems, after)
    return out[ns:]


def _gather_pass_on(rows, zones, name):
    ns = len(zones)

    def body(*refs):
        in_refs, out_refs = refs[:ns], refs[ns:2 * ns]
        send_sems, recv_sems = refs[2 * ns:]
        x, y, c = _position()
        copies = [pltpu.make_async_remote_copy(
            src_ref=_block_of(in_refs[s], 4 * cx + 2 * cy + c, rows[s]),
            dst_ref=_block_of(out_refs[s], 4 * cx + 2 * cy + c, rows[s]),
            send_sem=send_sems.at[j * ns + s], recv_sem=recv_sems.at[j * ns + s],
            device_id=(x, y, 1 - c), device_id_type=MESH_ID)
            for j, (cx, cy) in enumerate(_other_chips(x, y)) for s in range(ns)]
        for cp in copies:
            cp.start()
        for cp in copies:
            cp.wait_recv()
        for cp in copies:
            cp.wait_send()

    return pl.pallas_call(
        body, name=name,
        out_shape=tuple(jax.ShapeDtypeStruct(z.shape, z.dtype) for z in zones),
        in_specs=[ANY] * ns, out_specs=(ANY,) * ns,
        input_output_aliases={i: i for i in range(ns)},
        scratch_shapes=[pltpu.SemaphoreType.DMA((3 * ns,)), pltpu.SemaphoreType.DMA((3 * ns,))],
    )(*zones)


def _pair_exchange(sections, grads, name):
    ns = len(sections)

    def body(*refs):
        g_refs, land = refs[:ns], refs[ns:2 * ns]
        send_sems, recv_sems = refs[2 * ns:]
        x, y, c = _position()
        copies = [pltpu.make_async_remote_copy(
            src_ref=_block_of(g_refs[s], 2 * k + 1 - c, rows), dst_ref=land[s].at[k],
            send_sem=send_sems.at[k * ns + s], recv_sem=recv_sems.at[k * ns + s],
            device_id=(x, y, 1 - c), device_id_type=MESH_ID)
            for k in range(N_CHIPS) for s, (_, rows, _) in enumerate(sections)]
        for cp in copies:
            cp.start()
        for cp in copies:
            cp.wait_recv()
        for cp in copies:
            cp.wait_send()

    n = N_CHIPS * ns
    return pl.pallas_call(
        body, name=name,
        out_shape=tuple(jax.ShapeDtypeStruct((N_CHIPS, rows, cols), BF16) for _, rows, cols in sections),
        in_specs=[ANY] * ns, out_specs=(ANY,) * ns,
        scratch_shapes=[pltpu.SemaphoreType.DMA((n,)), pltpu.SemaphoreType.DMA((n,))],
    )(*grads)


def _pair_add(sections, grads, got, core, name):
    ns = len(sections)

    def body(core_ref, *refs):
        g_refs, got_refs, p_refs = refs[:ns], refs[ns:2 * ns], refs[2 * ns:]
        for s in range(ns):
            p_refs[s][0] = (g_refs[s][...].astype(F32) + got_refs[s][0].astype(F32)).astype(BF16)

    slot = [pl.BlockSpec((1, rows, cols), lambda k, c: (k, 0, 0)) for _, rows, cols in sections]
    return pl.pallas_call(
        body, name=name,
        out_shape=tuple(jax.ShapeDtypeStruct((N_CHIPS, rows, cols), BF16) for _, rows, cols in sections),
        grid_spec=pltpu.PrefetchScalarGridSpec(
            num_scalar_prefetch=1, grid=(N_CHIPS,),
            in_specs=[pl.BlockSpec((rows, cols), lambda k, c: (2 * k + c[0], 0)) for _, rows, cols in sections]
            + slot,
            out_specs=tuple(slot)),
        compiler_params=_params(dimension_semantics=("parallel",)),
    )(core, *grads, *got)


def _chip_copies(sections, p_refs, land, send_sems, recv_sems):
    ns = len(sections)
    x, y, c = _position()
    return [pltpu.make_async_remote_copy(
        src_ref=p_refs[s].at[2 * cx + cy], dst_ref=land[s].at[j],
        send_sem=send_sems.at[j * ns + s], recv_sem=recv_sems.at[j * ns + s],
        device_id=(cx, cy, c), device_id_type=MESH_ID)
        for j, (cx, cy) in enumerate(_other_chips(x, y)) for s in range(ns)]


def _chip_exchange(sections, parts, name):
    ns = len(sections)

    def body(*refs):
        copies = _chip_copies(sections, refs[:ns], refs[ns:2 * ns], *refs[2 * ns:])
        for cp in copies:
            cp.start()
        for cp in copies:
            cp.wait_recv()
        for cp in copies:
            cp.wait_send()

    n = 3 * ns
    return pl.pallas_call(
        body, name=name,
        out_shape=tuple(jax.ShapeDtypeStruct((3, rows, cols), BF16) for _, rows, cols in sections),
        in_specs=[ANY] * ns, out_specs=(ANY,) * ns,
        scratch_shapes=[pltpu.SemaphoreType.DMA((n,)), pltpu.SemaphoreType.DMA((n,))],
    )(*parts)


def _chip_exchange_start(sections, parts, name):
    ns = len(sections)

    def body(*refs):
        p_refs, land = refs[:ns], refs[ns:2 * ns]
        send_sems, recv_sems = refs[2 * ns], refs[2 * ns + 1]
        token = refs[-1]
        for cp in _chip_copies(sections, p_refs, land, send_sems, recv_sems):
            cp.start()
        token[...] = jnp.zeros_like(token)

    zones = [lax.empty((3, rows, cols), BF16) for _, rows, cols in sections]
    out = pl.pallas_call(
        body, name=name,
        out_shape=(pltpu.SemaphoreType.DMA((3 * ns,)), pltpu.SemaphoreType.DMA((3 * ns,)),
                   *[pltpu.HBM(a.shape, a.dtype) for a in parts], *[pltpu.HBM(a.shape, a.dtype) for a in zones],
                   jax.ShapeDtypeStruct((8, LANES), F32)),
        in_specs=[HBM] * (2 * ns),
        out_specs=(SEM, SEM, *[HBM] * (2 * ns), pl.BlockSpec(memory_space=pltpu.VMEM)),
        input_output_aliases={i: 2 + i for i in range(2 * ns)},
        compiler_params=pltpu.CompilerParams(has_side_effects=EFFECT),
    )(*[_in_hbm(a) for a in parts], *[_in_hbm(a) for a in zones])
    return out[0], out[1], out[2:2 + ns], out[2 + ns:2 + 2 * ns], out[-1]


def _chip_exchange_wait(sections, send_sems, recv_sems, parts, zones, after, name):
    ns = len(sections)

    def body(*refs):
        p_refs, land = refs[:ns], refs[ns:2 * ns]
        for cp in _chip_copies(sections, p_refs, land, refs[2 * ns], refs[2 * ns + 1]):
            cp.wait_send()
            cp.wait_recv()

    out = pl.pallas_call(
        body, name=name,
        out_shape=tuple(pltpu.HBM(a.shape, a.dtype) for a in (*parts, *zones)),
        in_specs=[HBM] * (2 * ns) + [SEM, SEM, ANY],
        out_specs=(HBM,) * (2 * ns),
        input_output_aliases={i: i for i in range(2 * ns)},
        compiler_params=pltpu.CompilerParams(has_side_effects=EFFECT),
    )(*parts, *zones, send_sems, recv_sems, after)
    return out[:ns], out[ns:]


def _grad_finish(sections, parts, far, chip, name):
    ns = len(sections)

    def body(chip_ref, *refs):
        p_refs, b_refs, g_refs = refs[:ns], refs[ns:2 * ns], refs[2 * ns:]
        for s in range(ns):
            g = p_refs[s][0].astype(F32)
            for j in range(3):
                g = g + b_refs[s][j].astype(F32)
            g_refs[s][...] = g

    half = [(rows // 2, cols) for _, rows, cols in sections]
    return pl.pallas_call(
        body, name=name,
        out_shape=tuple(jax.ShapeDtypeStruct((rows, cols), F32) for _, rows, cols in sections),
        grid_spec=pltpu.PrefetchScalarGridSpec(
            num_scalar_prefetch=1, grid=(2,),
            in_specs=[pl.BlockSpec((1, r, c), lambda i, chip: (chip[0], i, 0)) for r, c in half]
            + [pl.BlockSpec((3, r, c), lambda i, chip: (0, i, 0)) for r, c in half],
            out_specs=tuple(pl.BlockSpec((r, c), lambda i, chip: (i, 0)) for r, c in half)),
        compiler_params=_params(dimension_semantics=("parallel",)),
    )(chip, *parts, *far)


def _sum_devices(parts, rows, name):
    cols = parts.shape[1]
    tr = rows // 2

    def body(*refs):
        s = refs[0][...].astype(F32)
        for d in range(1, N_DEV):
            s = s + refs[d][...].astype(F32)
        refs[N_DEV][...] = s

    return pl.pallas_call(
        body, name=name,
        out_shape=jax.ShapeDtypeStruct((rows, cols), F32),
        grid=(2,),
        in_specs=[pl.BlockSpec((tr, cols), lambda i, d=d: (2 * d + i, 0)) for d in range(N_DEV)],
        out_specs=pl.BlockSpec((tr, cols), lambda i: (i, 0)),
        compiler_params=_params(dimension_semantics=("parallel",)),
    )(*([parts] * N_DEV))


def _adamw(w, g, m, v, name):
    rows, cols = w.shape
    tr = rows
    while tr * cols * 4 > (1 << 20) and tr % 16 == 0:
        tr //= 2
    c1 = 1.0 / (1.0 - ADAM_B1 ** ADAM_STEP)
    c2 = 1.0 / (1.0 - ADAM_B2 ** ADAM_STEP)

    def body(w_ref, g_ref, m_ref, v_ref, d_ref, nm_ref, nv_ref):
        gv = g_ref[...]
        nm = ADAM_B1 * m_ref[...] + (1.0 - ADAM_B1) * gv
        nv = ADAM_B2 * v_ref[...] + (1.0 - ADAM_B2) * (gv * gv)
        nm_ref[...] = nm
        nv_ref[...] = nv
        d_ref[...] = (-ADAM_LR) * ((nm * c1) / (jnp.sqrt(nv * c2) + ADAM_EPS) + ADAM_WD * w_ref[...])

    spec = pl.BlockSpec((tr, cols), lambda i: (i, 0))
    shape = jax.ShapeDtypeStruct((rows, cols), F32)
    return pl.pallas_call(
        body, name=name,
        out_shape=(shape, shape, shape),
        grid=(rows // tr,),
        in_specs=[spec] * 4, out_specs=(spec,) * 3,
        compiler_params=_params(dimension_semantics=("parallel",)),
    )(w, g, m, v)


NAMES = ("ln1_g", "w_in", "b_in", "rpb", "w_att_o", "conv_w", "conv_b", "w_rg_a", "b_rg_a", "w_rg_i",
         "b_rg_i", "lru_lambda", "w_rec_o", "w_out", "ln2_g", "w_ff1", "w_ff2", "lnf_g")
TRANSPOSED = {"w_in": "w_in_t", "w_att_o": "w_att_o_t", "w_ff1": "w_ff1_t"}
ROW_SHARDED = ("w_rec_o", "w_out", "w_ff2")
REPLICATED = (("ln1_g", (1, D)), ("b_in", (1, D_IN)), ("rpb", (N_HEADS * N_RPB_R, N_RPB_C)),
              ("conv_b", (1, D_REC)), ("w_rg_a", (2 * N_REC_BLOCKS * REC_BLOCK, REC_BLOCK)),
              ("w_rg_i", (2 * N_REC_BLOCKS * REC_BLOCK, REC_BLOCK)), ("ln2_g", (1, D)), ("lnf_g", (1, D)))
GATE_BLOCKS = ("w_rg_a", "w_rg_i")
SMALL_ROWS = 112


def _chan_bits(vectors):
    chan = jnp.concatenate(vectors, axis=0)
    bits = lax.bitcast_convert_type(chan, BF16).reshape(-1)
    return jnp.pad(bits, (0, CHAN_BLOCK_ROWS * D - bits.shape[0])).reshape(CHAN_BLOCK_ROWS, D)


def _chan_from_bits(gathered):
    bits = gathered.reshape(N_DEV, CHAN_BLOCK_ROWS * D)[:, :2 * N_CHAN_ROWS * LANES]
    chan = lax.bitcast_convert_type(bits.reshape(N_DEV, N_CHAN_ROWS, LANES, 2), F32)
    return chan.transpose(1, 0, 2).reshape(N_CHAN_ROWS, D)


def kernel(x, ln1_g, w_in, b_in, rpb, w_att_o, conv_w, conv_b, w_rg_a, b_rg_a, w_rg_i, b_rg_i, lru_lambda, w_rec_o, w_out, ln2_g, w_ff1, w_ff2, lnf_g, loss_target, m_ln1_g, m_w_in, m_b_in, m_rpb, m_w_att_o, m_conv_w, m_conv_b, m_w_rg_a, m_b_rg_a, m_w_rg_i, m_b_rg_i, m_lru_lambda, m_w_rec_o, m_w_out, m_ln2_g, m_w_ff1, m_w_ff2, m_lnf_g, v_ln1_g, v_w_in, v_b_in, v_rpb, v_w_att_o, v_conv_w, v_conv_b, v_w_rg_a, v_b_rg_a, v_w_rg_i, v_b_rg_i, v_lru_lambda, v_w_rec_o, v_w_out, v_ln2_g, v_w_ff1, v_w_ff2, v_lnf_g):
    w = dict(zip(NAMES, (ln1_g, w_in, b_in, rpb, w_att_o, conv_w, conv_b, w_rg_a, b_rg_a, w_rg_i,
                         b_rg_i, lru_lambda, w_rec_o, w_out, ln2_g, w_ff1, w_ff2, lnf_g)))
    m = dict(zip(NAMES, (m_ln1_g, m_w_in, m_b_in, m_rpb, m_w_att_o, m_conv_w, m_conv_b, m_w_rg_a,
                         m_b_rg_a, m_w_rg_i, m_b_rg_i, m_lru_lambda, m_w_rec_o, m_w_out, m_ln2_g,
                         m_w_ff1, m_w_ff2, m_lnf_g)))
    v = dict(zip(NAMES, (v_ln1_g, v_w_in, v_b_in, v_rpb, v_w_att_o, v_conv_w, v_conv_b, v_w_rg_a,
                         v_b_rg_a, v_w_rg_i, v_b_rg_i, v_lru_lambda, v_w_rec_o, v_w_out, v_ln2_g,
                         v_w_ff1, v_w_ff2, v_lnf_g)))
    xi, yi, ci = _position()

    shard = {t: w[n][0].T.astype(BF16) for n, t in TRANSPOSED.items()}
    shard.update({n: w[n][0].astype(BF16) for n in ROW_SHARDED})
    shard["chan"] = _chan_bits([w[n][0] for n, _ in CHAN])
    first, later = ("w_in_t", "chan"), ("w_rec_o", "w_out", "w_ff1_t", "w_ff2", "w_att_o_t")
    *gathered, done = _all_gather([shard[n] for n in first], "weight_all_gather")
    p = dict(zip(first, gathered))
    send_sems, recv_sems, sent, zones, token = _gather_start([shard[n] for n in later], done,
                                                             "weight_gather_start")

    def late_weights(after):
        landed = _gather_wait(send_sems, recv_sems, sent, zones, after, "weight_gather_wait")
        return dict(zip(later, _gather_pass_on([shard[n].shape[0] for n in later], landed,
                                               "weight_gather_pass_on")))

    chan = _chan_from_bits(p.pop("chan"))
    r0 = 0
    for n, rows in CHAN:
        p[n] = chan[r0:r0 + rows]
        r0 += rows
    p.update(ln1_g=w["ln1_g"], b_in=w["b_in"] + token[0, 0], rpb=w["rpb"][0], conv_b=w["conv_b"],
             w_rg_a=w["w_rg_a"][0], w_rg_i=w["w_rg_i"][0], ln2_g=w["ln2_g"],
             lnf_g=w["lnf_g"].reshape(1, D))

    core = jnp.reshape(ci, (1,)).astype(jnp.int32)
    chip = jnp.reshape(2 * xi + yi, (1,)).astype(jnp.int32)
    early_sections, late_sections = SECTIONS[1:], SECTIONS[:1]
    in_flight = {}

    def reduce_early(grads):
        chan_g = jnp.concatenate([grads[n] for n, _ in CHAN], axis=0)
        chan_g = chan_g.reshape(N_CHAN_ROWS, N_DEV, LANES).transpose(1, 0, 2).astype(BF16)
        chan_g = jnp.pad(chan_g.reshape(N_DEV, -1), ((0, 0), (0, CHAN_BLOCK_ROWS * D - N_CHAN_ROWS * LANES)))
        grads["chan"] = chan_g.reshape(N_DEV * CHAN_BLOCK_ROWS, D)
        grads["gates"] = jnp.concatenate([grads[n].reshape(-1, D) for n in GATE_BLOCKS], axis=0).astype(BF16)
        sect = [grads[n] for n, _, _ in early_sections]
        got = _pair_exchange(early_sections, sect, "grad_pair_exchange_early")
        parts = _pair_add(early_sections, sect, got, core, "grad_pair_add_early")
        in_flight["early"] = _chip_exchange_start(early_sections, parts, "grad_chip_exchange_start")
        return in_flight["early"][-1][0, 0]

    loss_part, grad_x, grads = _local_step(x[0], loss_target[0], p, late_weights, reduce_early)
    sect = [grads[n] for n, _, _ in late_sections]
    got = _pair_exchange(late_sections, sect, "grad_pair_exchange_late")
    late_parts = _pair_add(late_sections, sect, got, core, "grad_pair_add_late")
    in_flight["late"] = _chip_exchange_start(late_sections, late_parts, "grad_chip_exchange_start_late")

    def finish(group, sections, after, name):
        send_sems, recv_sems, parts, zones, _ = in_flight[group]
        parts, far = _chip_exchange_wait(sections, send_sems, recv_sems, parts, zones, after,
                                         "grad_chip_exchange_wait_" + name)
        return dict(zip((n for n, _, _ in sections),
                        _grad_finish(sections, parts, far, chip, "grad_finish_" + name)))

    started_late = in_flight["late"][-1]
    summed = finish("early", early_sections, started_late, "early")

    flat = jnp.concatenate([grads[n].reshape(-1) for n, _ in REPLICATED if n not in GATE_BLOCKS]
                           + [loss_part.reshape(-1) + started_late[0, 0]])
    n_small = flat.shape[0]
    flat = jnp.pad(flat, (0, SMALL_ROWS * LANES - n_small)).reshape(SMALL_ROWS, LANES)
    small_parts, gate_sum, _ = _all_gather([flat, summed["gates"]], "small_grad_all_gather")
    small = _sum_devices(small_parts, SMALL_ROWS, "small_grad_sum").reshape(-1)
    loss = small[n_small - 1]

    g, delta, new_m, new_v = {}, {}, {}, {}

    def update(n, g2, shape2):
        d2, m2, v2 = _adamw(w[n].reshape(shape2), g2, m[n].reshape(shape2), v[n].reshape(shape2),
                            "adamw_" + n)
        g[n], delta[n], new_m[n], new_v[n] = (a.reshape(w[n].shape) for a in (g2, d2, m2, v2))

    o = 0
    for n, shape2 in REPLICATED:
        if n in GATE_BLOCKS:
            k, rows = GATE_BLOCKS.index(n), gate_sum.shape[0] // len(GATE_BLOCKS)
            update(n, gate_sum[k * rows:(k + 1) * rows].reshape(shape2), shape2)
        else:
            size = shape2[0] * shape2[1]
            update(n, small[o:o + size].reshape(shape2), shape2)
            o += size

    for n in ROW_SHARDED:
        update(n, summed[n], summed[n].shape)
    for n, t in TRANSPOSED.items():
        if t in summed:
            update(n, summed[t].T, summed[t].shape[::-1])
    chan_back = summed["chan"].reshape(-1)[:N_CHAN_ROWS * LANES].reshape(N_CHAN_ROWS, LANES)
    r0 = 0
    for n, rows in CHAN:
        update(n, chan_back[r0:r0 + rows], (rows, LANES))
        r0 += rows
    all_updated = sum(delta[n].reshape(-1)[0] for n in delta).reshape(1, 1)
    summed = finish("late", late_sections, all_updated, "late")
    update("w_in", summed["w_in_t"].T, summed["w_in_t"].shape[::-1])

    return (loss, grad_x[None], *[g[n] for n in NAMES], *[delta[n] for n in NAMES],
            *[new_m[n] for n in NAMES], *[new_v[n] for n in NAMES])
```

```python
import math

import numpy as np
import jax
import jax.numpy as jnp
from jax import lax
from jax.experimental import pallas as pl
from jax.experimental.pallas import tpu as pltpu

F32 = jnp.float32
BF16 = jnp.bfloat16

T = 2048
D = 1024
D_ATT = 512
D_REC = 1024
D_FF = 4096
D_IN = 5632
N_HEADS = 8
DH = 64
GRID_W = 64
ROWS = T // GRID_W
WIN_H = 8
WIN_W = 16
KWIN = WIN_H * GRID_W
N_RPB_R = 2 * WIN_H - 1
N_RPB_C = 2 * WIN_W - 1
N_REC_BLOCKS = 16
REC_BLOCK = 64
CG = 128
N_CG = D_REC // CG
LRU_C = 8.0
EPS = 1e-6
N_DEV = 8
N_CHIPS = 4
LANES = 128

ADAM_LR = 0.001
ADAM_B1 = 0.9
ADAM_B2 = 0.999
ADAM_EPS = 1e-08
ADAM_WD = 0.01
ADAM_STEP = 10

MESH_AXES = ("x", "y", "c")
VMEM_LIMIT = 56 * 1024 * 1024

TILE = 512
DZ_ARRAYS = ((0, 3, 1), (3, 4, 2), (7, 4, 2))
N_DZ_TILES = D_IN // TILE


def _params(**kw):
    return pltpu.CompilerParams(vmem_limit_bytes=VMEM_LIMIT, **kw)


def _att_tables():
    rq = np.arange(2 * GRID_W) % GRID_W
    kc = np.arange(KWIN) % GRID_W
    win_start = np.clip(rq - WIN_W // 2, 0, GRID_W - WIN_W)
    valid = (kc[None, :] >= win_start[:, None]) & (kc[None, :] < win_start[:, None] + WIN_W)
    return valid.astype(np.float32), _pair_mask()


def _pair_mask():
    half = np.arange(2 * DH) // DH
    return (half[:, None] == half[None, :]).astype(np.float32)


def _dup_table():
    return np.concatenate([np.eye(REC_BLOCK, dtype=np.float32)] * 2, axis=1)


def _sigmoid(x):
    return 0.5 * jnp.tanh(0.5 * x) + 0.5


def _softplus(x):
    return jnp.maximum(x, 0.0) + jnp.log(1.0 + jnp.exp(-jnp.abs(x)))


def _one_minus_square(log_a, a):
    x = 2.0 * log_a
    series = -x * (1.0 + x * (0.5 + x * (1.0 / 6.0)))
    return jnp.where(x > -0.02, series, 1.0 - a * a)


_GELU_C = math.sqrt(2.0 / math.pi)


def _gelu_and_grad(x):
    x2 = x * x
    inner = _GELU_C * (x + 0.044715 * x * x2)
    t = jnp.tanh(inner)
    g = 0.5 * x * (1.0 + t)
    dg = 0.5 * (1.0 + t) + 0.5 * x * (1.0 - t * t) * _GELU_C * (1.0 + 3.0 * 0.044715 * x2)
    return g, dg


def _dot(a, b):
    return jnp.dot(a, b, preferred_element_type=F32)


def _dot_nt(a, b):
    return lax.dot_general(a, b, (((1,), (1,)), ((), ())), preferred_element_type=F32)


def _dot_tn(a, b):
    return lax.dot_general(a, b, (((0,), (0,)), ((), ())), preferred_element_type=F32)


def _dot_exact(a, b):
    return jnp.dot(a, b, precision=lax.Precision.HIGHEST, preferred_element_type=F32)


def _shift_rows(x, s):
    n = x.shape[0]
    rows = lax.broadcasted_iota(jnp.int32, x.shape, 0)
    y = pltpu.roll(x, s % n, 0)
    if s > 0:
        return jnp.where(rows >= s, y, 0.0)
    return jnp.where(rows < n + s, y, 0.0)


def _rms_bwd(dh, xh, r, g):
    dxh = dh * g
    return r * (dxh - xh * jnp.mean(dxh * xh, axis=-1, keepdims=True))


def _matmul(a, b, mode, out_dtype, name, tm=512, tn=1024, tk=2048):
    if mode == "nn":
        (m, k), (k2, n) = a.shape, b.shape
    elif mode == "nt":
        (m, k), (n, k2) = a.shape, b.shape
    else:
        (k, m), (k2, n) = a.shape, b.shape
    assert k == k2
    tm, tn, tk = min(tm, m), min(tn, n), min(tk, k)
    assert m % tm == 0 and n % tn == 0 and k % tk == 0
    nk = k // tk
    dot = {"nn": _dot, "nt": _dot_nt, "tn": _dot_tn}[mode]

    def body(a_ref, b_ref, o_ref, acc):
        kk = pl.program_id(2)
        part = dot(a_ref[...].astype(BF16), b_ref[...].astype(BF16))
        if nk == 1:
            o_ref[...] = part.astype(out_dtype)
            return

        @pl.when(kk == 0)
        def _():
            acc[...] = part

        @pl.when(kk > 0)
        def _():
            acc[...] += part

        @pl.when(kk == nk - 1)
        def _():
            o_ref[...] = acc[...].astype(out_dtype)

    if mode == "tn":
        a_spec = pl.BlockSpec((tk, tm), lambda i, j, kk: (kk, i))
    else:
        a_spec = pl.BlockSpec((tm, tk), lambda i, j, kk: (i, kk))
    if mode == "nt":
        b_spec = pl.BlockSpec((tn, tk), lambda i, j, kk: (j, kk))
    else:
        b_spec = pl.BlockSpec((tk, tn), lambda i, j, kk: (kk, j))
    return pl.pallas_call(
        body, name=name,
        out_shape=jax.ShapeDtypeStruct((m, n), out_dtype),
        grid=(m // tm, n // tn, nk),
        in_specs=[a_spec, b_spec],
        out_specs=pl.BlockSpec((tm, tn), lambda i, j, kk: (i, j)),
        scratch_shapes=[pltpu.VMEM((tm, tn) if nk > 1 else (8, LANES), F32)],
        compiler_params=_params(dimension_semantics=("parallel", "parallel", "arbitrary")),
    )(a, b)


def _in_proj(x, g1, w_in_t, b_in):
    tm = 1024

    def body(x_ref, g_ref, w_ref, b_ref, qkv_ref, uy_ref, gg_ref, h_ref, h_scr):
        j = pl.program_id(1)

        @pl.when(j == 0)
        def _():
            xv = x_ref[...]
            r = lax.rsqrt(jnp.mean(xv * xv, axis=-1, keepdims=True) + EPS)
            h = ((xv * r) * g_ref[...]).astype(BF16)
            h_scr[...] = h
            h_ref[...] = h

        z = _dot_nt(h_scr[...], w_ref[...]) + b_ref[...]

        @pl.when(j < 3)
        def _():
            qkv_ref[...] = z.astype(BF16)

        @pl.when((j >= 3) & (j < 7))
        def _():
            uy_ref[...] = z

        @pl.when(j >= 7)
        def _():
            gg_ref[...] = z

    return pl.pallas_call(
        body, name="in_proj",
        out_shape=(jax.ShapeDtypeStruct((T, 3 * D_ATT), BF16),
                   jax.ShapeDtypeStruct((T, 2 * D_REC), F32),
                   jax.ShapeDtypeStruct((T, 2 * D), F32),
                   jax.ShapeDtypeStruct((T, D), BF16)),
        grid=(T // tm, N_DZ_TILES),
        in_specs=[pl.BlockSpec((tm, D), lambda i, j: (i, 0)),
                  pl.BlockSpec((1, D), lambda i, j: (0, 0)),
                  pl.BlockSpec((TILE, D), lambda i, j: (j, 0)),
                  pl.BlockSpec((1, TILE), lambda i, j: (0, j))],
        out_specs=(pl.BlockSpec((tm, TILE), lambda i, j: (i, jnp.minimum(j, 2))),
                   pl.BlockSpec((tm, TILE), lambda i, j: (i, jnp.clip(j - 3, 0, 3))),
                   pl.BlockSpec((tm, TILE), lambda i, j: (i, jnp.clip(j - 7, 0, 3))),
                   pl.BlockSpec((tm, D), lambda i, j: (i, 0))),
        scratch_shapes=[pltpu.VMEM((tm, D), BF16)],
        compiler_params=_params(dimension_semantics=("parallel", "arbitrary")),
    )(x, g1, w_in_t, b_in)


def _dz_specs(rows, tile_of, row_of):
    def spec(off, n, per_plane):
        def index(*ids):
            t = jnp.clip(tile_of(*ids) - off, 0, n - 1)
            return (t // per_plane, row_of(*ids), t % per_plane)
        return pl.BlockSpec((1, rows, TILE), index)
    return [spec(off, n, per) for off, n, per in DZ_ARRAYS]


def _dh_norm1_bwd(dz, w_in_t, x, g1, dx1):
    tm = 1024

    def body(*refs):
        seg_refs = refs[:3]
        w_ref, x_ref, g_ref, dx1_ref, gx_ref, dg_ref, acc = refs[3:]
        i, kk = pl.program_id(0), pl.program_id(1)

        @pl.when(kk == 0)
        def _():
            acc[...] = jnp.zeros_like(acc)

        for s, (off, n, _) in enumerate(DZ_ARRAYS):
            @pl.when((kk >= off) & (kk < off + n))
            def _(s=s):
                acc[...] += _dot(seg_refs[s][0], w_ref[...])

        @pl.when((i == 0) & (kk == 0))
        def _():
            dg_ref[...] = jnp.zeros_like(dg_ref)

        @pl.when(kk == N_DZ_TILES - 1)
        def _():
            xv = x_ref[...]
            r = lax.rsqrt(jnp.mean(xv * xv, axis=-1, keepdims=True) + EPS)
            xh = xv * r
            dh = acc[...]
            dg_ref[...] += jnp.sum(dh * xh, axis=0, keepdims=True)
            gx_ref[...] = dx1_ref[...] + _rms_bwd(dh, xh, r, g_ref[...])

    tok = pl.BlockSpec((tm, D), lambda i, j: (i, 0))
    vec = pl.BlockSpec((1, D), lambda i, j: (0, 0))
    return pl.pallas_call(
        body, name="dh_norm1_bwd",
        out_shape=(jax.ShapeDtypeStruct((T, D), F32), jax.ShapeDtypeStruct((1, D), F32)),
        grid=(T // tm, N_DZ_TILES),
        in_specs=_dz_specs(tm, lambda i, j: j, lambda i, j: i)
        + [pl.BlockSpec((TILE, D), lambda i, j: (j, 0)), tok, vec, tok],
        out_specs=(tok, vec),
        scratch_shapes=[pltpu.VMEM((tm, D), F32)],
        compiler_params=_params(dimension_semantics=("arbitrary", "arbitrary")),
    )(*dz, w_in_t, x, g1, dx1)


def _grad_w_in(dz, h):
    def body(*refs):
        seg_refs = refs[:3]
        h_ref, gw_ref, gb_ref = refs[3:]
        j = pl.program_id(0)

        for s, (off, n, _) in enumerate(DZ_ARRAYS):
            @pl.when((j >= off) & (j < off + n))
            def _(s=s):
                a = seg_refs[s][0]
                gw_ref[...] = _dot_tn(a, h_ref[...]).astype(BF16)
                gb_ref[...] = jnp.sum(a.astype(F32), axis=0, keepdims=True)

    return pl.pallas_call(
        body, name="grad_w_in",
        out_shape=(jax.ShapeDtypeStruct((D_IN, D), BF16), jax.ShapeDtypeStruct((1, D_IN), F32)),
        grid=(N_DZ_TILES,),
        in_specs=_dz_specs(T, lambda j: j, lambda j: 0) + [pl.BlockSpec((T, D), lambda j: (0, 0))],
        out_specs=(pl.BlockSpec((TILE, D), lambda j: (j, 0)), pl.BlockSpec((1, TILE), lambda j: (0, j))),
        compiler_params=_params(dimension_semantics=("parallel",)),
    )(*dz, h)


def _rpb_rows(rpb):
    padded = jnp.pad(rpb, ((0, 0), (0, 0), (0, GRID_W - N_RPB_C)))
    rows = [padded[:, WIN_H - 1 - oi: 2 * WIN_H - 1 - oi].reshape(N_HEADS // 2, 2, KWIN)
            for oi in range(WIN_H)]
    return jnp.stack(rows, axis=0)


SKEW = KWIN - (WIN_W - 1)


MASKED = -1e30


def _bias_tiles(rows_ref, valid, bias_s):
    for oi in range(WIN_H):
        for hh in range(2):
            row = jnp.broadcast_to(rows_ref[oi, 0, hh:hh + 1, :], (GRID_W, KWIN))
            tile = pltpu.roll(row, SKEW, 1, stride=1, stride_axis=0)
            bias_s[oi, hh * GRID_W:(hh + 1) * GRID_W, :] = jnp.where(valid[:GRID_W], tile, MASKED)


def _bias_tile_grads(gb_s, flip, out_ref):
    for oi in range(WIN_H):
        for hh in range(2):
            g = _dot_exact(flip, gb_s[oi, hh * GRID_W:(hh + 1) * GRID_W, :])
            back = pltpu.roll(g, KWIN - (GRID_W - WIN_W), 1, stride=1, stride_axis=0)
            out_ref[0, oi, hh:hh + 1, :] = jnp.sum(back, axis=0, keepdims=True)


def _rpb_fold(row_grads):
    g = row_grads.transpose(1, 0, 2, 3).reshape(WIN_H, N_HEADS, WIN_H, GRID_W)
    g = g.transpose(0, 2, 1, 3)

    def body(g_ref, o_ref):
        for dr in range(N_RPB_R):
            terms = [g_ref[oi, i] for oi in range(WIN_H) for i in range(WIN_H) if i - oi + WIN_H - 1 == dr]
            acc = terms[0]
            for term in terms[1:]:
                acc = acc + term
            o_ref[dr] = acc

    out = pl.pallas_call(
        body, name="rpb_fold",
        out_shape=jax.ShapeDtypeStruct((N_RPB_R, N_HEADS, GRID_W), F32),
    )(g)
    return out.transpose(1, 0, 2)[:, :, :N_RPB_C]


def _att_scores(q_ref, k_ref, bias_ref, hmask, r):
    rs = jnp.clip(r - WIN_H // 2, 0, ROWS - WIN_H)
    oi = r - rs
    q0 = pl.multiple_of(r * GRID_W, GRID_W)
    k0 = pl.multiple_of(rs * GRID_W, GRID_W)
    q_r = q_ref[pl.ds(q0, GRID_W), :] * (DH ** -0.5)
    q2 = jnp.where(hmask, jnp.concatenate([q_r, q_r], axis=0), jnp.zeros((), BF16))
    kw = k_ref[pl.ds(k0, KWIN), :]
    s = _dot_nt(q2, kw) + bias_ref[oi]
    e = jnp.exp(s - jnp.max(s, axis=-1, keepdims=True))
    return e, 1.0 / jnp.sum(e, axis=-1, keepdims=True), q2, kw, q0, k0, oi


def _att_fwd(qkv, bias_rows):
    valid_np, hmask_np = _att_tables()

    def body(q_ref, k_ref, v_ref, rows_ref, valid_ref, hmask_ref, o_ref, bias_s):
        valid = valid_ref[...] > 0.5
        hmask = hmask_ref[...] > 0.5
        first_head = lax.broadcasted_iota(jnp.int32, (GRID_W, 2 * DH), 1) < DH
        _bias_tiles(rows_ref, valid, bias_s)

        def row(r, carry):
            e, rl, _, _, q0, k0, _ = _att_scores(q_ref, k_ref, bias_s, hmask, r)
            o2 = _dot((e * rl).astype(BF16), v_ref[pl.ds(k0, KWIN), :])
            o_ref[pl.ds(q0, GRID_W), :] = jnp.where(first_head, o2[:GRID_W], o2[GRID_W:]).astype(BF16)
            return carry

        lax.fori_loop(0, ROWS, row, 0, unroll=4)

    col = lambda off: pl.BlockSpec((T, 2 * DH), lambda hp: (0, hp + off))
    return pl.pallas_call(
        body, name="att_fwd",
        out_shape=jax.ShapeDtypeStruct((T, D_ATT), BF16),
        grid=(N_HEADS // 2,),
        in_specs=[col(0), col(4), col(8),
                  pl.BlockSpec((WIN_H, 1, 2, KWIN), lambda hp: (0, hp, 0, 0)),
                  pl.BlockSpec((2 * GRID_W, KWIN), lambda hp: (0, 0)),
                  pl.BlockSpec((2 * DH, 2 * DH), lambda hp: (0, 0))],
        out_specs=pl.BlockSpec((T, 2 * DH), lambda hp: (0, hp)),
        scratch_shapes=[pltpu.VMEM((WIN_H, 2 * GRID_W, KWIN), F32)],
        compiler_params=_params(dimension_semantics=("parallel",)),
    )(qkv, qkv, qkv, bias_rows, jnp.asarray(valid_np), jnp.asarray(hmask_np))


def _att_bwd(qkv, bias_rows, datt, after):
    valid_np, hmask_np = _att_tables()

    def body(q_ref, k_ref, v_ref, do_ref, rows_ref, valid_ref, hmask_ref, flip_ref,
             dqkv_ref, grows_ref, dk_acc, dv_acc, bias_s, gb_s):
        valid = valid_ref[...] > 0.5
        hmask = hmask_ref[...] > 0.5
        first_head = lax.broadcasted_iota(jnp.int32, (GRID_W, 2 * DH), 1) < DH
        dk_acc[...] = jnp.zeros_like(dk_acc)
        dv_acc[...] = jnp.zeros_like(dv_acc)
        gb_s[...] = jnp.zeros_like(gb_s)
        _bias_tiles(rows_ref, valid, bias_s)

        def row(r, carry):
            e, rl, q2, kw, q0, k0, oi = _att_scores(q_ref, k_ref, bias_s, hmask, r)
            do_r = do_ref[pl.ds(q0, GRID_W), :]
            do2 = jnp.where(hmask, jnp.concatenate([do_r, do_r], axis=0), jnp.zeros((), BF16))
            vw = v_ref[pl.ds(k0, KWIN), :]
            p = e * rl
            dp = _dot_nt(do2, vw)
            ds = p * (dp - jnp.sum(dp * p, axis=-1, keepdims=True))
            p16 = p.astype(BF16)
            ds16 = ds.astype(BF16)
            dv_acc[pl.ds(k0, KWIN), :] += _dot_tn(p16, do2)
            dk_acc[pl.ds(k0, KWIN), :] += _dot_tn(ds16, q2)
            dq2 = _dot(ds16, kw) * (DH ** -0.5)
            dqkv_ref[0, pl.ds(q0, GRID_W), :] = jnp.where(first_head, dq2[:GRID_W], dq2[GRID_W:]).astype(BF16)
            gb_s[oi] += ds
            return carry

        lax.fori_loop(0, ROWS, row, 0, unroll=4)
        dqkv_ref[1] = dk_acc[...].astype(BF16)
        dqkv_ref[2] = dv_acc[...].astype(BF16)
        _bias_tile_grads(gb_s, flip_ref[...], grows_ref)

    col = lambda off: pl.BlockSpec((T, 2 * DH), lambda hp: (0, hp + off))
    tiles = pltpu.VMEM((WIN_H, 2 * GRID_W, KWIN), F32)
    return pl.pallas_call(
        body, name="att_bwd",
        out_shape=(jax.ShapeDtypeStruct((3, T, D_ATT), BF16),
                   jax.ShapeDtypeStruct((N_HEADS // 2, WIN_H, 2, KWIN), F32)),
        grid=(N_HEADS // 2,),
        in_specs=[col(0), col(4), col(8), col(0),
                  pl.BlockSpec((WIN_H, 1, 2, KWIN), lambda hp: (0, hp, 0, 0)),
                  pl.BlockSpec((2 * GRID_W, KWIN), lambda hp: (0, 0)),
                  pl.BlockSpec((2 * DH, 2 * DH), lambda hp: (0, 0)),
                  pl.BlockSpec((GRID_W, GRID_W), lambda hp: (0, 0))],
        out_specs=(pl.BlockSpec((3, T, 2 * DH), lambda hp: (0, 0, hp)),
                   pl.BlockSpec((1, WIN_H, 2, KWIN), lambda hp: (hp, 0, 0, 0))),
        scratch_shapes=[pltpu.VMEM((T, 2 * DH), F32), pltpu.VMEM((T, 2 * DH), F32), tiles, tiles],
        compiler_params=_params(dimension_semantics=("parallel",)),
    )(qkv, qkv, qkv, datt, bias_rows, jnp.asarray(valid_np) + after, jnp.asarray(hmask_np),
      jnp.asarray(np.eye(GRID_W, dtype=np.float32)[::-1].copy()))


def _conv_taps(up):
    return (_shift_rows(up, 2), _shift_rows(up, 1), up, _shift_rows(up, -1))


def _pair_block_diag(w_pair, dup, same_half):
    return jnp.where(same_half, _dot(w_pair.astype(BF16), dup), 0.0).astype(BF16)


def _gates(u, u16, wa, ba, wi, bi, lam):
    r = _sigmoid(_dot(u16, wa) + ba)
    ig = _sigmoid(_dot(u16, wi) + bi)
    sp = _softplus(-lam)
    log_a = (-LRU_C) * r * sp
    a = jnp.exp(log_a)
    mult2 = jnp.maximum(_one_minus_square(log_a, a), 0.0)
    return r, ig, sp, a, jnp.sqrt(mult2), mult2


SCAN_BLOCKS = 2


def _scans(jobs):
    c = jobs[0][0].shape[1]
    nblk = T // 8
    rows = lax.broadcasted_iota(jnp.int32, (8, c), 0)

    def block(a, b, reverse):
        for s in (1, 2, 4):
            if reverse:
                keep = rows < 8 - s
                a_s = jnp.where(keep, pltpu.roll(a, 8 - s, 0), 1.0)
                b_s = jnp.where(keep, pltpu.roll(b, 8 - s, 0), 0.0)
            else:
                keep = rows >= s
                a_s = jnp.where(keep, pltpu.roll(a, s, 0), 1.0)
                b_s = jnp.where(keep, pltpu.roll(b, s, 0), 0.0)
            b = a * b_s + b
            a = a * a_s
        return a, b

    def step(i, carry):
        out = []
        for (a_ref, b_ref, h_ref, reverse), h_prev in zip(jobs, carry):
            for u in range(SCAN_BLOCKS):
                blk = i * SCAN_BLOCKS + u
                if reverse:
                    blk = nblk - 1 - blk
                t0 = pl.multiple_of(blk * 8, 8)
                a, b = block(a_ref[pl.ds(t0, 8), :], b_ref[pl.ds(t0, 8), :], reverse)
                h = a * h_prev + b
                h_ref[pl.ds(t0, 8), :] = h
                h_prev = jnp.broadcast_to(h[0:1] if reverse else h[7:8], (8, c))
            out.append(h_prev)
        return tuple(out)

    lax.fori_loop(0, nblk // SCAN_BLOCKS, step, tuple(jnp.zeros((8, c), F32) for _ in jobs))


def _rec_specs():
    tok = lambda off: pl.BlockSpec((T, CG), lambda g: (0, g + off))
    per_ch = lambda rows: pl.BlockSpec((rows, CG), lambda g: (0, g))
    wspec = pl.BlockSpec((2, 1, CG, REC_BLOCK), lambda g: (0, g, 0, 0))
    const = lambda shape: pl.BlockSpec(shape, lambda g: (0, 0))
    return tok, per_ch, wspec, const


def _rec_fwd(uy, conv_w, conv_b, w_a, b_a, w_i, b_i, lam):
    tok, per_ch, wspec, const = _rec_specs()

    def body(up_ref, yb_ref, cw_ref, cb_ref, wa_ref, ba_ref, wi_ref, bi_ref, lam_ref, dup_ref, half_ref,
             hf_ref, hb_ref, yrec_ref, a_f, bx_f, a_b, bx_b):
        dup = dup_ref[...]
        same_half = half_ref[...] > 0.5
        taps = _conv_taps(up_ref[...])
        u = cb_ref[...]
        for j in range(4):
            u = u + taps[j] * cw_ref[j:j + 1, :]
        u16 = u.astype(BF16)
        for d, (a_s, bx_s) in enumerate(((a_f, bx_f), (a_b, bx_b))):
            wa = _pair_block_diag(wa_ref[d, 0], dup, same_half)
            wi = _pair_block_diag(wi_ref[d, 0], dup, same_half)
            _, ig, _, a, mult, _ = _gates(u, u16, wa, ba_ref[d:d + 1, :], wi, bi_ref[d:d + 1, :],
                                       lam_ref[d:d + 1, :])
            a_s[...] = a
            bx_s[...] = mult * (ig * u)
        _scans([(a_f, bx_f, hf_ref, False), (a_b, bx_b, hb_ref, True)])
        gelu, _ = _gelu_and_grad(yb_ref[...])
        yrec_ref[...] = ((hf_ref[...] + hb_ref[...]) * gelu).astype(BF16)

    return pl.pallas_call(
        body, name="rec_fwd",
        out_shape=(jax.ShapeDtypeStruct((T, D_REC), F32), jax.ShapeDtypeStruct((T, D_REC), F32),
                   jax.ShapeDtypeStruct((T, D_REC), BF16)),
        grid=(N_CG,),
        in_specs=[tok(0), tok(N_CG), per_ch(4), per_ch(1), wspec, per_ch(2), wspec, per_ch(2), per_ch(2),
                  const((REC_BLOCK, CG)), const((CG, CG))],
        out_specs=(tok(0), tok(0), tok(0)),
        scratch_shapes=[pltpu.VMEM((T, CG), F32)] * 4,
        compiler_params=_params(dimension_semantics=("parallel",)),
    )(uy, uy, conv_w, conv_b, w_a, b_a, w_i, b_i, lam,
      jnp.asarray(_dup_table(), BF16), jnp.asarray(_pair_mask()))


def _rec_bwd(uy, hf, hb, dyrec, conv_w, conv_b, w_a, b_a, w_i, b_i, lam):
    tok, per_ch, wspec, const = _rec_specs()

    def body(up_ref, yb_ref, hf_ref, hb_ref, dy_ref, cw_ref, cb_ref, wa_ref, ba_ref, wi_ref, bi_ref,
             lam_ref, dup_ref, dupt_ref, half_ref,
             duy_ref, dcw_ref, dcb_ref, dwa_ref, dba_ref, dwi_ref, dbi_ref, dlam_ref,
             a_s0, a_s1, dh_s, g_s0, g_s1):
        dup = dup_ref[...]
        dup_t = dupt_ref[...]
        same_half = half_ref[...] > 0.5
        taps = _conv_taps(up_ref[...])
        u = cb_ref[...]
        for j in range(4):
            u = u + taps[j] * cw_ref[j:j + 1, :]
        u16 = u.astype(BF16)
        gelu, dgelu = _gelu_and_grad(yb_ref[...])
        dy = dy_ref[...]
        duy_ref[1] = (dy * (hf_ref[...] + hb_ref[...]) * dgelu).astype(BF16)
        dh_s[...] = dy * gelu
        gate_values = []
        for d, a_s in enumerate((a_s0, a_s1)):
            wa = _pair_block_diag(wa_ref[d, 0], dup, same_half)
            wi = _pair_block_diag(wi_ref[d, 0], dup, same_half)
            lam_d = lam_ref[d:d + 1, :]
            r, ig, sp, a, mult, mult2 = _gates(u, u16, wa, ba_ref[d:d + 1, :], wi, bi_ref[d:d + 1, :], lam_d)
            a_s[...] = _shift_rows(a, 1 if d == 1 else -1)
            gate_values.append((wa, wi, lam_d, r, ig, sp, a, mult, mult2))
        _scans([(a_s0, dh_s, g_s0, True), (a_s1, dh_s, g_s1, False)])
        du = jnp.zeros((T, CG), F32)
        for d, g_s in enumerate((g_s0, g_s1)):
            reverse = d == 1
            wa, wi, lam_d, r, ig, sp, a, mult, mult2 = gate_values[d]
            g = g_s[...]
            h_prev = _shift_rows(hb_ref[...], -1) if reverse else _shift_rows(hf_ref[...], 1)
            da = g * h_prev
            dmult = g * (ig * u)
            dig = g * mult * u
            du = du + g * mult * ig
            dmult_dlog = jnp.where(mult2 > 0.0, -(a * a) * lax.rsqrt(mult2), 0.0)
            dlog_a = da * a + dmult * dmult_dlog
            dr = dlog_a * ((-LRU_C) * sp)
            dsp = jnp.sum(dlog_a * ((-LRU_C) * r), axis=0, keepdims=True)
            dlam_ref[d:d + 1, :] = dsp * (-_sigmoid(-lam_d))
            dga = dr * r * (1.0 - r)
            dgi = dig * ig * (1.0 - ig)
            dga16 = dga.astype(BF16)
            dgi16 = dgi.astype(BF16)
            du = du + _dot_nt(dga16, wa) + _dot_nt(dgi16, wi)
            dwa_ref[d, 0] = _dot_exact(jnp.where(same_half, _dot_tn(u16, dga16), 0.0), dup_t)
            dwi_ref[d, 0] = _dot_exact(jnp.where(same_half, _dot_tn(u16, dgi16), 0.0), dup_t)
            dba_ref[d:d + 1, :] = jnp.sum(dga, axis=0, keepdims=True)
            dbi_ref[d:d + 1, :] = jnp.sum(dgi, axis=0, keepdims=True)
        dcb_ref[...] = jnp.sum(du, axis=0, keepdims=True)
        for j in range(4):
            dcw_ref[j:j + 1, :] = jnp.sum(du * taps[j], axis=0, keepdims=True)
        dup_in = (_shift_rows(du, -2) * cw_ref[0:1, :] + _shift_rows(du, -1) * cw_ref[1:2, :]
                  + du * cw_ref[2:3, :] + _shift_rows(du, 1) * cw_ref[3:4, :])
        duy_ref[0] = dup_in.astype(BF16)

    wshape = jax.ShapeDtypeStruct((2, N_CG, CG, REC_BLOCK), F32)
    vec = lambda rows: jax.ShapeDtypeStruct((rows, D_REC), F32)
    dup_np = _dup_table()
    return pl.pallas_call(
        body, name="rec_bwd",
        out_shape=(jax.ShapeDtypeStruct((2, T, D_REC), BF16),
                   vec(4), vec(1), wshape, vec(2), wshape, vec(2), vec(2)),
        grid=(N_CG,),
        in_specs=[tok(0), tok(N_CG), tok(0), tok(0), tok(0),
                  per_ch(4), per_ch(1), wspec, per_ch(2), wspec, per_ch(2), per_ch(2),
                  const((REC_BLOCK, CG)), const((CG, REC_BLOCK)), const((CG, CG))],
        out_specs=(pl.BlockSpec((2, T, CG), lambda g: (0, 0, g)),
                   per_ch(4), per_ch(1), wspec, per_ch(2), wspec, per_ch(2), per_ch(2)),
        scratch_shapes=[pltpu.VMEM((T, CG), F32)] * 5,
        compiler_params=_params(dimension_semantics=("parallel",)),
    )(uy, uy, hf, hb, dyrec, conv_w, conv_b, w_a, b_a, w_i, b_i, lam,
      jnp.asarray(dup_np, BF16), jnp.asarray(dup_np.T.copy()), jnp.asarray(_pair_mask()))


TM_MIX = 256


def _mix_specs():
    tok = lambda width, blk=0: pl.BlockSpec((TM_MIX, width), lambda i: (i, blk))
    full = lambda shape: pl.BlockSpec(shape, lambda i: (0, 0))
    return tok, full


def _mix_fwd(x, att, yrec, gg, w_att_o_t, w_rec_o, w_out):
    tok, full = _mix_specs()

    def body(x_ref, att_ref, yr_ref, ga_ref, gr_ref, wao_ref, wro_ref, wo_ref, x1_ref, mixed_ref):
        y_att = _dot_nt(att_ref[...], wao_ref[...])
        y_rec = _dot(yr_ref[...], wro_ref[...])
        mixed = (_sigmoid(ga_ref[...]) * y_att + _sigmoid(gr_ref[...]) * y_rec).astype(BF16)
        mixed_ref[...] = mixed
        x1_ref[...] = x_ref[...] + _dot(mixed, wo_ref[...])

    return pl.pallas_call(
        body, name="mix_fwd",
        out_shape=(jax.ShapeDtypeStruct((T, D), F32), jax.ShapeDtypeStruct((T, D), BF16)),
        grid=(T // TM_MIX,),
        in_specs=[tok(D), tok(D_ATT), tok(D_REC), tok(D, 0), tok(D, 1),
                  full((D, D_ATT)), full((D_REC, D)), full((D, D))],
        out_specs=(tok(D), tok(D)),
        compiler_params=_params(dimension_semantics=("parallel",)),
    )(x, att, yrec, gg, gg, w_att_o_t, w_rec_o, w_out)


def _mix_bwd(dx1, att, yrec, gg, w_att_o_t, w_rec_o, w_out):
    tok, full = _mix_specs()

    def body(dx_ref, att_ref, yr_ref, ga_ref, gr_ref, wao_ref, wro_ref, wo_ref,
             dgg_ref, dya_ref, dyr_ref, datt_ref, dyrp_ref):
        dmixed = _dot_nt(dx_ref[...].astype(BF16), wo_ref[...])
        y_att = _dot_nt(att_ref[...], wao_ref[...])
        y_rec = _dot(yr_ref[...], wro_ref[...])
        sa = _sigmoid(ga_ref[...])
        sr = _sigmoid(gr_ref[...])
        dgg_ref[0] = (dmixed * y_att * sa * (1.0 - sa)).astype(BF16)
        dgg_ref[1] = (dmixed * y_rec * sr * (1.0 - sr)).astype(BF16)
        dya = (dmixed * sa).astype(BF16)
        dyr = (dmixed * sr).astype(BF16)
        dya_ref[...] = dya
        dyr_ref[...] = dyr
        datt_ref[...] = _dot(dya, wao_ref[...]).astype(BF16)
        dyrp_ref[...] = _dot_nt(dyr, wro_ref[...])

    return pl.pallas_call(
        body, name="mix_bwd",
        out_shape=(jax.ShapeDtypeStruct((2, T, D), BF16),
                   jax.ShapeDtypeStruct((T, D), BF16), jax.ShapeDtypeStruct((T, D), BF16),
                   jax.ShapeDtypeStruct((T, D_ATT), BF16), jax.ShapeDtypeStruct((T, D_REC), F32)),
        grid=(T // TM_MIX,),
        in_specs=[tok(D), tok(D_ATT), tok(D_REC), tok(D, 0), tok(D, 1),
                  full((D, D_ATT)), full((D_REC, D)), full((D, D))],
        out_specs=(pl.BlockSpec((2, TM_MIX, D), lambda i: (0, i, 0)),
                   tok(D), tok(D), tok(D_ATT), tok(D_REC)),
        compiler_params=_params(dimension_semantics=("parallel",)),
    )(dx1, att, yrec, gg, gg, w_att_o_t, w_rec_o, w_out)


TM_FFN = 256
FF_CHUNK = 1024


def _ffn_loss(x1, target, g2, gf, w_ff1_t, w_ff2):
    n_chunks = D_FF // FF_CHUNK

    def body(x1_ref, tg_ref, g2_ref, gf_ref, w1_hbm, w2_hbm,
             loss_ref, dx1_ref, h2_ref, act_ref, dpre_ref, dx2_ref, dg2_ref, dgf_ref,
             w1, w2, relu_s):
        i = pl.program_id(0)

        @pl.when(i == 0)
        def _():
            pltpu.sync_copy(w1_hbm, w1)
            pltpu.sync_copy(w2_hbm, w2)
            loss_ref[...] = jnp.zeros_like(loss_ref)
            dg2_ref[...] = jnp.zeros_like(dg2_ref)
            dgf_ref[...] = jnp.zeros_like(dgf_ref)

        x1v = x1_ref[...]
        r2 = lax.rsqrt(jnp.mean(x1v * x1v, axis=-1, keepdims=True) + EPS)
        xh2 = x1v * r2
        h2 = (xh2 * g2_ref[...]).astype(BF16)
        h2_ref[...] = h2
        x2 = x1v
        for c in range(n_chunks):
            ff = slice(c * FF_CHUNK, (c + 1) * FF_CHUNK)
            rl = jnp.maximum(_dot_nt(h2, w1[ff, :]), 0.0)
            relu_s[:, ff] = rl
            act = (rl * rl).astype(BF16)
            act_ref[:, ff] = act
            x2 = x2 + _dot(act, w2[ff, :])
        r3 = lax.rsqrt(jnp.mean(x2 * x2, axis=-1, keepdims=True) + EPS)
        xh3 = x2 * r3
        err = xh3 * gf_ref[...] - tg_ref[...]
        loss_ref[...] += 0.5 * jnp.sum(jnp.mean(err * err, axis=-1, keepdims=True))
        dy = err * (1.0 / D)
        dgf_ref[...] += jnp.sum(dy * xh3, axis=0, keepdims=True)
        dx2 = _rms_bwd(dy, xh3, r3, gf_ref[...])
        dx2_16 = dx2.astype(BF16)
        dx2_ref[...] = dx2_16
        dh2 = jnp.zeros((TM_FFN, D), F32)
        for c in range(n_chunks):
            ff = slice(c * FF_CHUNK, (c + 1) * FF_CHUNK)
            dpre = (_dot_nt(dx2_16, w2[ff, :]) * (2.0 * relu_s[:, ff])).astype(BF16)
            dpre_ref[:, ff] = dpre
            dh2 = dh2 + _dot(dpre, w1[ff, :])
        dg2_ref[...] += jnp.sum(dh2 * xh2, axis=0, keepdims=True)
        dx1_ref[...] = dx2 + _rms_bwd(dh2, xh2, r2, g2_ref[...])

    tok = lambda width: pl.BlockSpec((TM_FFN, width), lambda i: (i, 0))
    vec = pl.BlockSpec((1, D), lambda i: (0, 0))
    hbm = pl.BlockSpec(memory_space=pl.ANY)
    return pl.pallas_call(
        body, name="ffn_loss",
        out_shape=(jax.ShapeDtypeStruct((8, 128), F32), jax.ShapeDtypeStruct((T, D), F32),
                   jax.ShapeDtypeStruct((T, D), BF16), jax.ShapeDtypeStruct((T, D_FF), BF16),
                   jax.ShapeDtypeStruct((T, D_FF), BF16), jax.ShapeDtypeStruct((T, D), BF16),
                   jax.ShapeDtypeStruct((1, D), F32), jax.ShapeDtypeStruct((1, D), F32)),
        grid=(T // TM_FFN,),
        in_specs=[tok(D), tok(D), vec, vec, hbm, hbm],
        out_specs=(pl.BlockSpec((8, 128), lambda i: (0, 0)), tok(D), tok(D), tok(D_FF), tok(D_FF), tok(D),
                   vec, vec),
        scratch_shapes=[pltpu.VMEM((D_FF, D), BF16), pltpu.VMEM((D_FF, D), BF16),
                        pltpu.VMEM((TM_FFN, D_FF), F32)],
        compiler_params=_params(dimension_semantics=("arbitrary",)),
    )(x1, target, g2, gf, w_ff1_t, w_ff2)


def _local_step(x, target, p, late_weights, reduce_early):
    bias = _rpb_rows(p["rpb"])
    pairs = lambda w: w.reshape(2, N_CG, CG, REC_BLOCK)
    w_a, w_i = pairs(p["w_rg_a"]), pairs(p["w_rg_i"])
    rec_params = (p["conv_w"], p["conv_b"], w_a, p["b_rg_a"], w_i, p["b_rg_i"], p["lru_lambda"])

    qkv, uy, gg, h = _in_proj(x, p["ln1_g"], p["w_in_t"], p["b_in"])
    att = _att_fwd(qkv, bias)
    hf, hb, yrec = _rec_fwd(uy, *rec_params)
    p = {**p, **late_weights(yrec)}
    x1, mixed = _mix_fwd(x, att, yrec, gg, p["w_att_o_t"], p["w_rec_o"], p["w_out"])
    loss8, dx1, h2, act, dpre, dx2, g_ln2, g_lnf = _ffn_loss(
        x1, target, p["ln2_g"], p["lnf_g"], p["w_ff1_t"], p["w_ff2"])

    dgg, dya, dyr, datt, dyrp = _mix_bwd(dx1, att, yrec, gg, p["w_att_o_t"], p["w_rec_o"], p["w_out"])
    duy, g_cw, g_cb, g_wa, g_ba, g_wi, g_bi, g_lam = _rec_bwd(uy, hf, hb, dyrp, *rec_params)
    blocks = lambda g: g.reshape(2, N_REC_BLOCKS, REC_BLOCK, REC_BLOCK)
    grads = {
        "w_att_o_t": _matmul(dya, att, "tn", BF16, "g_w_att_o"),
        "conv_w": g_cw, "conv_b": g_cb, "w_rg_a": blocks(g_wa), "b_rg_a": g_ba,
        "w_rg_i": blocks(g_wi), "b_rg_i": g_bi, "lru_lambda": g_lam,
        "w_rec_o": _matmul(yrec, dyr, "tn", BF16, "g_w_rec_o"),
        "w_out": _matmul(mixed, dx1, "tn", BF16, "g_w_out"),
        "ln2_g": g_ln2,
        "w_ff1_t": _matmul(dpre, h2, "tn", BF16, "g_w_ff1"),
        "w_ff2": _matmul(act, dx2, "tn", BF16, "g_w_ff2"),
        "lnf_g": g_lnf,
    }
    after = reduce_early(grads)
    dqkv, gbias = _att_bwd(qkv, bias, datt, after)
    dz = (dqkv, duy, dgg)
    grad_x, g_ln1 = _dh_norm1_bwd(dz, p["w_in_t"], x, p["ln1_g"], dx1)
    g_w_in_t, g_b_in = _grad_w_in(dz, h)
    grads.update(ln1_g=g_ln1, w_in_t=g_w_in_t, b_in=g_b_in, rpb=_rpb_fold(gbias))
    return loss8[0:1, 0:1], grad_x, grads


MESH_ID = pl.DeviceIdType.MESH
ANY = pl.BlockSpec(memory_space=pl.ANY)

CHAN_BLOCK_ROWS = 32
GATE_ROWS = 2 * 2 * N_REC_BLOCKS * REC_BLOCK * REC_BLOCK // (N_DEV * D)
SECTIONS = (("w_in_t", 704, D), ("w_rec_o", 128, D), ("w_out", 128, D), ("w_ff1_t", 512, D),
            ("w_ff2", 512, D), ("chan", CHAN_BLOCK_ROWS, D), ("w_att_o_t", 128, D_ATT),
            ("gates", GATE_ROWS, D))
N_SEC = len(SECTIONS)
N_CHAN_ROWS = 10
CHAN = (("conv_w", 4), ("b_rg_a", 2), ("b_rg_i", 2), ("lru_lambda", 2))


def _position():
    return lax.axis_index("x"), lax.axis_index("y"), lax.axis_index("c")


def _other_chips(x, y):
    return [(1 - x, y), (x, 1 - y), (1 - x, 1 - y)]


def _block_of(ref, dev, rows):
    return ref.at[pl.ds(pl.multiple_of(dev * rows, 16), rows)]


def _all_gather(shards, name):
    ns = len(shards)

    def body(*refs):
        x_refs, out_refs, done_ref = refs[:ns], refs[ns:2 * ns], refs[2 * ns]
        send_sems, recv_sems, local_sems = refs[2 * ns + 1:]
        done_ref[0, 0] = 0.0
        x, y, c = _position()
        me, sibling = (x, y, c), (x, y, 1 - c)
        chips = _other_chips(x, y)

        def rows(s, px, py, pc):
            return _block_of(out_refs[s], 4 * px + 2 * py + pc, shards[s].shape[0])

        def copy(k, s, block, to, from_shard=False):
            return pltpu.make_async_remote_copy(
                src_ref=x_refs[s] if from_shard else rows(s, *block), dst_ref=rows(s, *block),
                send_sem=send_sems.at[k * ns + s], recv_sem=recv_sems.at[k * ns + s],
                device_id=to, device_id_type=MESH_ID)

        sections = range(ns)
        mine = [pltpu.make_async_copy(x_refs[s], rows(s, *me), local_sems.at[s]) for s in sections]
        first = [copy(0, s, me, sibling, True) for s in sections]
        first += [copy(1 + j, s, me, (*chip, c), True) for j, chip in enumerate(chips) for s in sections]
        for cp in mine + first:
            cp.start()
        passed = []
        for j, chip in enumerate(chips):
            for s in sections:
                copy(1 + j, s, (*chip, c), me).wait_recv()
                passed.append(copy(4 + j, s, (*chip, c), sibling))
                passed[-1].start()
        for s in sections:
            copy(0, s, sibling, me).wait_recv()
        for j, chip in enumerate(chips):
            for s in sections:
                copy(4 + j, s, (*chip, 1 - c), me).wait_recv()
        for cp in first + passed:
            cp.wait_send()
        for cp in mine:
            cp.wait()

    return pl.pallas_call(
        body, name=name,
        out_shape=tuple(jax.ShapeDtypeStruct((N_DEV * s.shape[0], s.shape[1]), s.dtype) for s in shards)
        + (jax.ShapeDtypeStruct((1, 1), F32),),
        in_specs=[ANY] * ns,
        out_specs=(ANY,) * ns + (pl.BlockSpec(memory_space=pltpu.SMEM),),
        scratch_shapes=[pltpu.SemaphoreType.DMA((7 * ns,)), pltpu.SemaphoreType.DMA((7 * ns,)),
                        pltpu.SemaphoreType.DMA((ns,))],
    )(*shards)


HBM = pl.BlockSpec(memory_space=pltpu.HBM)
SEM = pl.BlockSpec(memory_space=pltpu.SEMAPHORE)
EFFECT = pltpu.SideEffectType.DATAFLOW_SIDE_EFFECTING


def _in_hbm(a):
    return pltpu.with_memory_space_constraint(a, pltpu.HBM)


def _first_hop_copies(shards, x_refs, zones, send_sems, recv_sems):
    ns = len(shards)
    x, y, c = _position()
    targets = [(x, y, 1 - c)] + [(cx, cy, c) for cx, cy in _other_chips(x, y)]
    return [pltpu.make_async_remote_copy(
        src_ref=x_refs[s], dst_ref=_block_of(zones[s], 4 * x + 2 * y + c, shards[s].shape[0]),
        send_sem=send_sems.at[k * ns + s], recv_sem=recv_sems.at[k * ns + s],
        device_id=to, device_id_type=MESH_ID)
        for k, to in enumerate(targets) for s in range(ns)]


def _after_all(arrays, name):
    def body(*refs):
        refs[-1][...] = jnp.zeros_like(refs[-1])

    return pl.pallas_call(
        body, name=name,
        out_shape=jax.ShapeDtypeStruct((8, LANES), F32),
        in_specs=[pl.BlockSpec(memory_space=pl.ANY)] * len(arrays),
        out_specs=pl.BlockSpec(memory_space=pltpu.VMEM),
    )(*arrays)


def _own_blocks_placed(shards, after):
    ns = len(shards)
    x, y, c = _position()
    me = jnp.reshape(4 * x + 2 * y + c, (1,)).astype(jnp.int32)
    shards = [*shards[:-1], shards[-1] + after.astype(shards[-1].dtype)]

    def body(me_ref, *refs):
        for s in range(ns):
            refs[ns + s][...] = refs[s][...]

    return pl.pallas_call(
        body, name="own_blocks_placed",
        out_shape=tuple(jax.ShapeDtypeStruct((N_DEV * s.shape[0], s.shape[1]), s.dtype) for s in shards),
        grid_spec=pltpu.PrefetchScalarGridSpec(
            num_scalar_prefetch=1, grid=(1,),
            in_specs=[pl.BlockSpec(s.shape, lambda i, me: (0, 0)) for s in shards],
            out_specs=tuple(pl.BlockSpec(s.shape, lambda i, me: (me[0], 0)) for s in shards)),
        compiler_params=_params(dimension_semantics=("arbitrary",)),
    )(me, *shards)


def _gather_start(shards, after, name):
    ns = len(shards)
    zones = _own_blocks_placed(shards, after)

    def body(*refs):
        for cp in _first_hop_copies(shards, refs[:ns], refs[ns:2 * ns], refs[2 * ns], refs[2 * ns + 1]):
            cp.start()
        refs[-1][...] = jnp.zeros_like(refs[-1])

    out = pl.pallas_call(
        body, name=name,
        out_shape=(pltpu.SemaphoreType.DMA((4 * ns,)), pltpu.SemaphoreType.DMA((4 * ns,)),
                   *[pltpu.HBM(a.shape, a.dtype) for a in (*shards, *zones)],
                   jax.ShapeDtypeStruct((8, LANES), F32)),
        in_specs=[HBM] * (2 * ns),
        out_specs=(SEM, SEM, *[HBM] * (2 * ns), pl.BlockSpec(memory_space=pltpu.VMEM)),
        input_output_aliases={i: 2 + i for i in range(2 * ns)},
        compiler_params=pltpu.CompilerParams(has_side_effects=EFFECT),
    )(*[_in_hbm(a) for a in shards], *[_in_hbm(a) for a in zones])
    return out[0], out[1], out[2:2 + ns], out[2 + ns:2 + 2 * ns], out[-1]


def _gather_wait(send_sems, recv_sems, shards, zones, after, name):
    ns = len(shards)

    def body(*refs):
        for cp in _first_hop_copies(shards, refs[:ns], refs[ns:2 * ns], refs[2 * ns], refs[2 * ns + 1]):
            cp.wait_send()
            cp.wait_recv()

    out = pl.pallas_call(
        body, name=name,
        out_shape=tuple(pltpu.HBM(a.shape, a.dtype) for a in (*shards, *zones)),
        in_specs=[HBM] * (2 * ns) + [SEM, SEM, ANY],
        out_specs=(HBM,) * (2 * ns),
        input_output_aliases={i: i for i in range(2 * ns)},
        compiler_params=pltpu.CompilerParams(has_side_effects=EFFECT),
    )(*shards, *zones, send_sems, recv_sems, after)
    return out[ns:]


def _gather_pass_on(rows, zones, name):
    ns = len(zones)

    def body(*refs):
        in_refs, out_refs = refs[:ns], refs[ns:2 * ns]
        send_sems, recv_sems = refs[2 * ns:]
        x, y, c = _position()
        copies = [pltpu.make_async_remote_copy(
            src_ref=_block_of(in_refs[s], 4 * cx + 2 * cy + c, rows[s]),
            dst_ref=_block_of(out_refs[s], 4 * cx + 2 * cy + c, rows[s]),
            send_sem=send_sems.at[j * ns + s], recv_sem=recv_sems.at[j * ns + s],
            device_id=(x, y, 1 - c), device_id_type=MESH_ID)
            for j, (cx, cy) in enumerate(_other_chips(x, y)) for s in range(ns)]
        for cp in copies:
            cp.start()
        for cp in copies:
            cp.wait_recv()
        for cp in copies:
            cp.wait_send()

    return pl.pallas_call(
        body, name=name,
        out_shape=tuple(jax.ShapeDtypeStruct(z.shape, z.dtype) for z in zones),
        in_specs=[ANY] * ns, out_specs=(ANY,) * ns,
        input_output_aliases={i: i for i in range(ns)},
        scratch_shapes=[pltpu.SemaphoreType.DMA((3 * ns,)), pltpu.SemaphoreType.DMA((3 * ns,))],
    )(*zones)


def _pair_exchange(sections, grads, name):
    ns = len(sections)

    def body(*refs):
        g_refs, land = refs[:ns], refs[ns:2 * ns]
        send_sems, recv_sems = refs[2 * ns:]
        x, y, c = _position()
        copies = [pltpu.make_async_remote_copy(
            src_ref=_block_of(g_refs[s], 2 * k + 1 - c, rows), dst_ref=land[s].at[k],
            send_sem=send_sems.at[k * ns + s], recv_sem=recv_sems.at[k * ns + s],
            device_id=(x, y, 1 - c), device_id_type=MESH_ID)
            for k in range(N_CHIPS) for s, (_, rows, _) in enumerate(sections)]
        for cp in copies:
            cp.start()
        for cp in copies:
            cp.wait_recv()
        for cp in copies:
            cp.wait_send()

    n = N_CHIPS * ns
    return pl.pallas_call(
        body, name=name,
        out_shape=tuple(jax.ShapeDtypeStruct((N_CHIPS, rows, cols), BF16) for _, rows, cols in sections),
        in_specs=[ANY] * ns, out_specs=(ANY,) * ns,
        scratch_shapes=[pltpu.SemaphoreType.DMA((n,)), pltpu.SemaphoreType.DMA((n,))],
    )(*grads)


def _pair_add(sections, grads, got, core, name):
    ns = len(sections)

    def body(core_ref, *refs):
        g_refs, got_refs, p_refs = refs[:ns], refs[ns:2 * ns], refs[2 * ns:]
        for s in range(ns):
            p_refs[s][0] = (g_refs[s][...].astype(F32) + got_refs[s][0].astype(F32)).astype(BF16)

    slot = [pl.BlockSpec((1, rows, cols), lambda k, c: (k, 0, 0)) for _, rows, cols in sections]
    return pl.pallas_call(
        body, name=name,
        out_shape=tuple(jax.ShapeDtypeStruct((N_CHIPS, rows, cols), BF16) for _, rows, cols in sections),
        grid_spec=pltpu.PrefetchScalarGridSpec(
            num_scalar_prefetch=1, grid=(N_CHIPS,),
            in_specs=[pl.BlockSpec((rows, cols), lambda k, c: (2 * k + c[0], 0)) for _, rows, cols in sections]
            + slot,
            out_specs=tuple(slot)),
        compiler_params=_params(dimension_semantics=("parallel",)),
    )(core, *grads, *got)


def _chip_copies(sections, p_refs, land, send_sems, recv_sems):
    ns = len(sections)
    x, y, c = _position()
    return [pltpu.make_async_remote_copy(
        src_ref=p_refs[s].at[2 * cx + cy], dst_ref=land[s].at[j],
        send_sem=send_sems.at[j * ns + s], recv_sem=recv_sems.at[j * ns + s],
        device_id=(cx, cy, c), device_id_type=MESH_ID)
        for j, (cx, cy) in enumerate(_other_chips(x, y)) for s in range(ns)]


def _chip_exchange(sections, parts, name):
    ns = len(sections)

    def body(*refs):
        copies = _chip_copies(sections, refs[:ns], refs[ns:2 * ns], *refs[2 * ns:])
        for cp in copies:
            cp.start()
        for cp in copies:
            cp.wait_recv()
        for cp in copies:
            cp.wait_send()

    n = 3 * ns
    return pl.pallas_call(
        body, name=name,
        out_shape=tuple(jax.ShapeDtypeStruct((3, rows, cols), BF16) for _, rows, cols in sections),
        in_specs=[ANY] * ns, out_specs=(ANY,) * ns,
        scratch_shapes=[pltpu.SemaphoreType.DMA((n,)), pltpu.SemaphoreType.DMA((n,))],
    )(*parts)


def _chip_exchange_start(sections, parts, name):
    ns = len(sections)

    def body(*refs):
        p_refs, land = refs[:ns], refs[ns:2 * ns]
        send_sems, recv_sems = refs[2 * ns], refs[2 * ns + 1]
        token = refs[-1]
        for cp in _chip_copies(sections, p_refs, land, send_sems, recv_sems):
            cp.start()
        token[...] = jnp.zeros_like(token)

    zones = [lax.empty((3, rows, cols), BF16) for _, rows, cols in sections]
    out = pl.pallas_call(
        body, name=name,
        out_shape=(pltpu.SemaphoreType.DMA((3 * ns,)), pltpu.SemaphoreType.DMA((3 * ns,)),
                   *[pltpu.HBM(a.shape, a.dtype) for a in parts], *[pltpu.HBM(a.shape, a.dtype) for a in zones],
                   jax.ShapeDtypeStruct((8, LANES), F32)),
        in_specs=[HBM] * (2 * ns),
        out_specs=(SEM, SEM, *[HBM] * (2 * ns), pl.BlockSpec(memory_space=pltpu.VMEM)),
        input_output_aliases={i: 2 + i for i in range(2 * ns)},
        compiler_params=pltpu.CompilerParams(has_side_effects=EFFECT),
    )(*[_in_hbm(a) for a in parts], *[_in_hbm(a) for a in zones])
    return out[0], out[1], out[2:2 + ns], out[2 + ns:2 + 2 * ns], out[-1]


def _chip_exchange_wait(sections, send_sems, recv_sems, parts, zones, after, name):
    ns = len(sections)

    def body(*refs):
        p_refs, land = refs[:ns], refs[ns:2 * ns]
        for cp in _chip_copies(sections, p_refs, land, refs[2 * ns], refs[2 * ns + 1]):
            cp.wait_send()
            cp.wait_recv()

    out = pl.pallas_call(
        body, name=name,
        out_shape=tuple(pltpu.HBM(a.shape, a.dtype) for a in (*parts, *zones)),
        in_specs=[HBM] * (2 * ns) + [SEM, SEM, ANY],
        out_specs=(HBM,) * (2 * ns),
        input_output_aliases={i: i for i in range(2 * ns)},
        compiler_params=pltpu.CompilerParams(has_side_effects=EFFECT),
    )(*parts, *zones, send_sems, recv_sems, after)
    return out[:ns], out[ns:]


def _grad_finish(sections, parts, far, chip, name):
    ns = len(sections)

    def body(chip_ref, *refs):
        p_refs, b_refs, g_refs = refs[:ns], refs[ns:2 * ns], refs[2 * ns:]
        for s in range(ns):
            g = p_refs[s][0].astype(F32)
            for j in range(3):
                g = g + b_refs[s][j].astype(F32)
            g_refs[s][...] = g

    half = [(rows // 2, cols) for _, rows, cols in sections]
    return pl.pallas_call(
        body, name=name,
        out_shape=tuple(jax.ShapeDtypeStruct((rows, cols), F32) for _, rows, cols in sections),
        grid_spec=pltpu.PrefetchScalarGridSpec(
            num_scalar_prefetch=1, grid=(2,),
            in_specs=[pl.BlockSpec((1, r, c), lambda i, chip: (chip[0], i, 0)) for r, c in half]
            + [pl.BlockSpec((3, r, c), lambda i, chip: (0, i, 0)) for r, c in half],
            out_specs=tuple(pl.BlockSpec((r, c), lambda i, chip: (i, 0)) for r, c in half)),
        compiler_params=_params(dimension_semantics=("parallel",)),
    )(chip, *parts, *far)


def _sum_devices(parts, rows, name):
    cols = parts.shape[1]
    tr = rows // 2

    def body(*refs):
        s = refs[0][...].astype(F32)
        for d in range(1, N_DEV):
            s = s + refs[d][...].astype(F32)
        refs[N_DEV][...] = s

    return pl.pallas_call(
        body, name=name,
        out_shape=jax.ShapeDtypeStruct((rows, cols), F32),
        grid=(2,),
        in_specs=[pl.BlockSpec((tr, cols), lambda i, d=d: (2 * d + i, 0)) for d in range(N_DEV)],
        out_specs=pl.BlockSpec((tr, cols), lambda i: (i, 0)),
        compiler_params=_params(dimension_semantics=("parallel",)),
    )(*([parts] * N_DEV))


def _adamw_step(w_ref, g_ref, m_ref, v_ref, d_ref, nm_ref, nv_ref):
    c1 = 1.0 / (1.0 - ADAM_B1 ** ADAM_STEP)
    c2 = 1.0 / (1.0 - ADAM_B2 ** ADAM_STEP)
    gv = g_ref[...]
    nm = ADAM_B1 * m_ref[...] + (1.0 - ADAM_B1) * gv
    nv = ADAM_B2 * v_ref[...] + (1.0 - ADAM_B2) * (gv * gv)
    nm_ref[...] = nm
    nv_ref[...] = nv
    d_ref[...] = (-ADAM_LR) * ((nm * c1) / (jnp.sqrt(nv * c2) + ADAM_EPS) + ADAM_WD * w_ref[...])


def _adamw_small(params, name):
    n = len(params)

    def body(*refs):
        for k in range(n):
            _adamw_step(*refs[4 * k:4 * k + 4], *refs[4 * n + 3 * k:4 * n + 3 * k + 3])

    out = pl.pallas_call(
        body, name=name,
        out_shape=tuple(jax.ShapeDtypeStruct(p[0].shape, F32) for p in params for _ in range(3)),
    )(*[a for p in params for a in p])
    return [out[3 * k:3 * k + 3] for k in range(n)]


def _adamw(w, g, m, v, name):
    rows, cols = w.shape
    tr = rows
    while tr * cols * 4 > (1 << 20) and tr % 16 == 0:
        tr //= 2

    def body(*refs):
        _adamw_step(*refs)

    spec = pl.BlockSpec((tr, cols), lambda i: (i, 0))
    shape = jax.ShapeDtypeStruct((rows, cols), F32)
    return pl.pallas_call(
        body, name=name,
        out_shape=(shape, shape, shape),
        grid=(rows // tr,),
        in_specs=[spec] * 4, out_specs=(spec,) * 3,
        compiler_params=_params(dimension_semantics=("parallel",)),
    )(w, g, m, v)


NAMES = ("ln1_g", "w_in", "b_in", "rpb", "w_att_o", "conv_w", "conv_b", "w_rg_a", "b_rg_a", "w_rg_i",
         "b_rg_i", "lru_lambda", "w_rec_o", "w_out", "ln2_g", "w_ff1", "w_ff2", "lnf_g")
TRANSPOSED = {"w_in": "w_in_t", "w_att_o": "w_att_o_t", "w_ff1": "w_ff1_t"}
ROW_SHARDED = ("w_rec_o", "w_out", "w_ff2")
REPLICATED = (("ln1_g", (1, D)), ("b_in", (1, D_IN)), ("rpb", (N_HEADS * N_RPB_R, N_RPB_C)),
              ("conv_b", (1, D_REC)), ("w_rg_a", (2 * N_REC_BLOCKS * REC_BLOCK, REC_BLOCK)),
              ("w_rg_i", (2 * N_REC_BLOCKS * REC_BLOCK, REC_BLOCK)), ("ln2_g", (1, D)), ("lnf_g", (1, D)))
GATE_BLOCKS = ("w_rg_a", "w_rg_i")
SMALL_ROWS = 112


def _chan_bits(vectors):
    chan = jnp.concatenate(vectors, axis=0)
    bits = lax.bitcast_convert_type(chan, BF16).reshape(-1)
    return jnp.pad(bits, (0, CHAN_BLOCK_ROWS * D - bits.shape[0])).reshape(CHAN_BLOCK_ROWS, D)


def _chan_from_bits(gathered):
    bits = gathered.reshape(N_DEV, CHAN_BLOCK_ROWS * D)[:, :2 * N_CHAN_ROWS * LANES]
    chan = lax.bitcast_convert_type(bits.reshape(N_DEV, N_CHAN_ROWS, LANES, 2), F32)
    return chan.transpose(1, 0, 2).reshape(N_CHAN_ROWS, D)


def kernel(x, ln1_g, w_in, b_in, rpb, w_att_o, conv_w, conv_b, w_rg_a, b_rg_a, w_rg_i, b_rg_i, lru_lambda, w_rec_o, w_out, ln2_g, w_ff1, w_ff2, lnf_g, loss_target, m_ln1_g, m_w_in, m_b_in, m_rpb, m_w_att_o, m_conv_w, m_conv_b, m_w_rg_a, m_b_rg_a, m_w_rg_i, m_b_rg_i, m_lru_lambda, m_w_rec_o, m_w_out, m_ln2_g, m_w_ff1, m_w_ff2, m_lnf_g, v_ln1_g, v_w_in, v_b_in, v_rpb, v_w_att_o, v_conv_w, v_conv_b, v_w_rg_a, v_b_rg_a, v_w_rg_i, v_b_rg_i, v_lru_lambda, v_w_rec_o, v_w_out, v_ln2_g, v_w_ff1, v_w_ff2, v_lnf_g):
    w = dict(zip(NAMES, (ln1_g, w_in, b_in, rpb, w_att_o, conv_w, conv_b, w_rg_a, b_rg_a, w_rg_i,
                         b_rg_i, lru_lambda, w_rec_o, w_out, ln2_g, w_ff1, w_ff2, lnf_g)))
    m = dict(zip(NAMES, (m_ln1_g, m_w_in, m_b_in, m_rpb, m_w_att_o, m_conv_w, m_conv_b, m_w_rg_a,
                         m_b_rg_a, m_w_rg_i, m_b_rg_i, m_lru_lambda, m_w_rec_o, m_w_out, m_ln2_g,
                         m_w_ff1, m_w_ff2, m_lnf_g)))
    v = dict(zip(NAMES, (v_ln1_g, v_w_in, v_b_in, v_rpb, v_w_att_o, v_conv_w, v_conv_b, v_w_rg_a,
                         v_b_rg_a, v_w_rg_i, v_b_rg_i, v_lru_lambda, v_w_rec_o, v_w_out, v_ln2_g,
                         v_w_ff1, v_w_ff2, v_lnf_g)))
    xi, yi, ci = _position()

    shard = {t: w[n][0].T.astype(BF16) for n, t in TRANSPOSED.items()}
    shard.update({n: w[n][0].astype(BF16) for n in ROW_SHARDED})
    shard["chan"] = _chan_bits([w[n][0] for n, _ in CHAN])
    first, later = ("w_in_t", "chan"), ("w_rec_o", "w_out", "w_ff1_t", "w_ff2", "w_att_o_t")
    *gathered, done = _all_gather([shard[n] for n in first], "weight_all_gather")
    p = dict(zip(first, gathered))
    send_sems, recv_sems, sent, zones, token = _gather_start([shard[n] for n in later], done,
                                                             "weight_gather_start")

    def late_weights(after):
        landed = _gather_wait(send_sems, recv_sems, sent, zones, after, "weight_gather_wait")
        return dict(zip(later, _gather_pass_on([shard[n].shape[0] for n in later], landed,
                                               "weight_gather_pass_on")))

    chan = _chan_from_bits(p.pop("chan"))
    r0 = 0
    for n, rows in CHAN:
        p[n] = chan[r0:r0 + rows]
        r0 += rows
    p.update(ln1_g=w["ln1_g"], b_in=w["b_in"] + token[0, 0], rpb=w["rpb"][0], conv_b=w["conv_b"],
             w_rg_a=w["w_rg_a"][0], w_rg_i=w["w_rg_i"][0], ln2_g=w["ln2_g"],
             lnf_g=w["lnf_g"].reshape(1, D))

    core = jnp.reshape(ci, (1,)).astype(jnp.int32)
    chip = jnp.reshape(2 * xi + yi, (1,)).astype(jnp.int32)
    early_sections, late_sections = SECTIONS[1:], SECTIONS[:1]
    in_flight = {}

    def reduce_early(grads):
        chan_g = jnp.concatenate([grads[n] for n, _ in CHAN], axis=0)
        chan_g = chan_g.reshape(N_CHAN_ROWS, N_DEV, LANES).transpose(1, 0, 2).astype(BF16)
        chan_g = jnp.pad(chan_g.reshape(N_DEV, -1), ((0, 0), (0, CHAN_BLOCK_ROWS * D - N_CHAN_ROWS * LANES)))
        grads["chan"] = chan_g.reshape(N_DEV * CHAN_BLOCK_ROWS, D)
        grads["gates"] = jnp.concatenate([grads[n].reshape(-1, D) for n in GATE_BLOCKS], axis=0).astype(BF16)
        sect = [grads[n] for n, _, _ in early_sections]
        got = _pair_exchange(early_sections, sect, "grad_pair_exchange_early")
        parts = _pair_add(early_sections, sect, got, core, "grad_pair_add_early")
        in_flight["early"] = _chip_exchange_start(early_sections, parts, "grad_chip_exchange_start")
        return in_flight["early"][-1][0, 0]

    loss_part, grad_x, grads = _local_step(x[0], loss_target[0], p, late_weights, reduce_early)
    sect = [grads[n] for n, _, _ in late_sections]
    got = _pair_exchange(late_sections, sect, "grad_pair_exchange_late")
    late_parts = _pair_add(late_sections, sect, got, core, "grad_pair_add_late")
    in_flight["late"] = _chip_exchange_start(late_sections, late_parts, "grad_chip_exchange_start_late")

    def finish(group, sections, after, name):
        send_sems, recv_sems, parts, zones, _ = in_flight[group]
        parts, far = _chip_exchange_wait(sections, send_sems, recv_sems, parts, zones, after,
                                         "grad_chip_exchange_wait_" + name)
        return dict(zip((n for n, _, _ in sections),
                        _grad_finish(sections, parts, far, chip, "grad_finish_" + name)))

    started_late = in_flight["late"][-1]
    summed = finish("early", early_sections, started_late, "early")

    flat = jnp.concatenate([grads[n].reshape(-1) for n, _ in REPLICATED if n not in GATE_BLOCKS]
                           + [loss_part.reshape(-1) + started_late[0, 0]])
    n_small = flat.shape[0]
    flat = jnp.pad(flat, (0, SMALL_ROWS * LANES - n_small)).reshape(SMALL_ROWS, LANES)
    small_parts, gate_sum, _ = _all_gather([flat, summed["gates"]], "small_grad_all_gather")
    small = _sum_devices(small_parts, SMALL_ROWS, "small_grad_sum").reshape(-1)
    loss = small[n_small - 1]

    g, delta, new_m, new_v = {}, {}, {}, {}

    def update(n, g2, shape2):
        d2, m2, v2 = _adamw(w[n].reshape(shape2), g2, m[n].reshape(shape2), v[n].reshape(shape2),
                            "adamw_" + n)
        g[n], delta[n], new_m[n], new_v[n] = (a.reshape(w[n].shape) for a in (g2, d2, m2, v2))

    small_params = []
    o = 0
    for n, shape2 in REPLICATED:
        if n in GATE_BLOCKS:
            k, rows = GATE_BLOCKS.index(n), gate_sum.shape[0] // len(GATE_BLOCKS)
            update(n, gate_sum[k * rows:(k + 1) * rows].reshape(shape2), shape2)
        else:
            size = shape2[0] * shape2[1]
            small_params.append((n, small[o:o + size].reshape(shape2), shape2))
            o += size
    chan_back = summed["chan"].reshape(-1)[:N_CHAN_ROWS * LANES].reshape(N_CHAN_ROWS, LANES)
    r0 = 0
    for n, rows in CHAN:
        small_params.append((n, chan_back[r0:r0 + rows], (rows, LANES)))
        r0 += rows
    results = _adamw_small([(w[n].reshape(s2), g2, m[n].reshape(s2), v[n].reshape(s2))
                            for n, g2, s2 in small_params], "adamw_vectors")
    for (n, g2, _), (d2, m2, v2) in zip(small_params, results):
        g[n], delta[n], new_m[n], new_v[n] = (a.reshape(w[n].shape) for a in (g2, d2, m2, v2))

    for n in ROW_SHARDED:
        update(n, summed[n], summed[n].shape)
    for n, t in TRANSPOSED.items():
        if t in summed:
            update(n, summed[t].T, summed[t].shape[::-1])
    summed = finish("late", late_sections, _after_all(list(delta.values()), "updates_done"), "late")
    update("w_in", summed["w_in_t"].T, summed["w_in_t"].shape[::-1])

    return (loss, grad_x[None], *[g[n] for n in NAMES], *[delta[n] for n in NAMES],
            *[new_m[n] for n in NAMES], *[new_v[n] for n in NAMES])
```

```python
import math

import numpy as np
import jax
import jax.numpy as jnp
from jax import lax
from jax.experimental import pallas as pl
from jax.experimental.pallas import tpu as pltpu

F32 = jnp.float32
BF16 = jnp.bfloat16

T = 2048
D = 1024
D_ATT = 512
D_REC = 1024
D_FF = 4096
D_IN = 5632
N_HEADS = 8
DH = 64
GRID_W = 64
ROWS = T // GRID_W
WIN_H = 8
WIN_W = 16
KWIN = WIN_H * GRID_W
N_RPB_R = 2 * WIN_H - 1
N_RPB_C = 2 * WIN_W - 1
N_REC_BLOCKS = 16
REC_BLOCK = 64
CG = 128
N_CG = D_REC // CG
LRU_C = 8.0
EPS = 1e-6
N_DEV = 8
N_CHIPS = 4
LANES = 128

ADAM_LR = 0.001
ADAM_B1 = 0.9
ADAM_B2 = 0.999
ADAM_EPS = 1e-08
ADAM_WD = 0.01
ADAM_STEP = 10

MESH_AXES = ("x", "y", "c")
VMEM_LIMIT = 56 * 1024 * 1024

TILE = 512
DZ_ARRAYS = ((0, 3, 1), (3, 4, 2), (7, 4, 2))
N_DZ_TILES = D_IN // TILE


def _params(**kw):
    return pltpu.CompilerParams(vmem_limit_bytes=VMEM_LIMIT, **kw)


def _att_tables():
    rq = np.arange(2 * GRID_W) % GRID_W
    kc = np.arange(KWIN) % GRID_W
    win_start = np.clip(rq - WIN_W // 2, 0, GRID_W - WIN_W)
    valid = (kc[None, :] >= win_start[:, None]) & (kc[None, :] < win_start[:, None] + WIN_W)
    return valid.astype(np.float32), _pair_mask()


def _pair_mask():
    half = np.arange(2 * DH) // DH
    return (half[:, None] == half[None, :]).astype(np.float32)


def _dup_table():
    return np.concatenate([np.eye(REC_BLOCK, dtype=np.float32)] * 2, axis=1)


def _sigmoid(x):
    return 0.5 * jnp.tanh(0.5 * x) + 0.5


def _softplus(x):
    return jnp.maximum(x, 0.0) + jnp.log(1.0 + jnp.exp(-jnp.abs(x)))


def _one_minus_square(log_a, a):
    x = 2.0 * log_a
    series = -x * (1.0 + x * (0.5 + x * (1.0 / 6.0)))
    return jnp.where(x > -0.02, series, 1.0 - a * a)


_GELU_C = math.sqrt(2.0 / math.pi)


def _gelu_and_grad(x):
    x2 = x * x
    inner = _GELU_C * (x + 0.044715 * x * x2)
    t = jnp.tanh(inner)
    g = 0.5 * x * (1.0 + t)
    dg = 0.5 * (1.0 + t) + 0.5 * x * (1.0 - t * t) * _GELU_C * (1.0 + 3.0 * 0.044715 * x2)
    return g, dg


def _dot(a, b):
    return jnp.dot(a, b, preferred_element_type=F32)


def _dot_nt(a, b):
    return lax.dot_general(a, b, (((1,), (1,)), ((), ())), preferred_element_type=F32)


def _dot_tn(a, b):
    return lax.dot_general(a, b, (((0,), (0,)), ((), ())), preferred_element_type=F32)


def _dot_exact(a, b):
    return jnp.dot(a, b, precision=lax.Precision.HIGHEST, preferred_element_type=F32)


def _shift_rows(x, s):
    n = x.shape[0]
    rows = lax.broadcasted_iota(jnp.int32, x.shape, 0)
    y = pltpu.roll(x, s % n, 0)
    if s > 0:
        return jnp.where(rows >= s, y, 0.0)
    return jnp.where(rows < n + s, y, 0.0)


def _rms_bwd(dh, xh, r, g):
    dxh = dh * g
    return r * (dxh - xh * jnp.mean(dxh * xh, axis=-1, keepdims=True))


def _matmul(a, b, mode, out_dtype, name, tm=512, tn=1024, tk=2048):
    if mode == "nn":
        (m, k), (k2, n) = a.shape, b.shape
    elif mode == "nt":
        (m, k), (n, k2) = a.shape, b.shape
    else:
        (k, m), (k2, n) = a.shape, b.shape
    assert k == k2
    tm, tn, tk = min(tm, m), min(tn, n), min(tk, k)
    assert m % tm == 0 and n % tn == 0 and k % tk == 0
    nk = k // tk
    dot = {"nn": _dot, "nt": _dot_nt, "tn": _dot_tn}[mode]

    def body(a_ref, b_ref, o_ref, acc):
        kk = pl.program_id(2)
        part = dot(a_ref[...].astype(BF16), b_ref[...].astype(BF16))
        if nk == 1:
            o_ref[...] = part.astype(out_dtype)
            return

        @pl.when(kk == 0)
        def _():
            acc[...] = part

        @pl.when(kk > 0)
        def _():
            acc[...] += part

        @pl.when(kk == nk - 1)
        def _():
            o_ref[...] = acc[...].astype(out_dtype)

    if mode == "tn":
        a_spec = pl.BlockSpec((tk, tm), lambda i, j, kk: (kk, i))
    else:
        a_spec = pl.BlockSpec((tm, tk), lambda i, j, kk: (i, kk))
    if mode == "nt":
        b_spec = pl.BlockSpec((tn, tk), lambda i, j, kk: (j, kk))
    else:
        b_spec = pl.BlockSpec((tk, tn), lambda i, j, kk: (kk, j))
    return pl.pallas_call(
        body, name=name,
        out_shape=jax.ShapeDtypeStruct((m, n), out_dtype),
        grid=(m // tm, n // tn, nk),
        in_specs=[a_spec, b_spec],
        out_specs=pl.BlockSpec((tm, tn), lambda i, j, kk: (i, j)),
        scratch_shapes=[pltpu.VMEM((tm, tn) if nk > 1 else (8, LANES), F32)],
        compiler_params=_params(dimension_semantics=("parallel", "parallel", "arbitrary")),
    )(a, b)


def _in_proj(x, g1, w_in_t, b_in):
    tm = 1024

    def body(x_ref, g_ref, w_ref, b_ref, qkv_ref, uy_ref, gg_ref, h_ref, h_scr):
        j = pl.program_id(1)

        @pl.when(j == 0)
        def _():
            xv = x_ref[...]
            r = lax.rsqrt(jnp.mean(xv * xv, axis=-1, keepdims=True) + EPS)
            h = ((xv * r) * g_ref[...]).astype(BF16)
            h_scr[...] = h
            h_ref[...] = h

        z = _dot_nt(h_scr[...], w_ref[...]) + b_ref[...]

        @pl.when(j < 3)
        def _():
            qkv_ref[...] = z.astype(BF16)

        @pl.when((j >= 3) & (j < 7))
        def _():
            uy_ref[...] = z

        @pl.when(j >= 7)
        def _():
            gg_ref[...] = z

    return pl.pallas_call(
        body, name="in_proj",
        out_shape=(jax.ShapeDtypeStruct((T, 3 * D_ATT), BF16),
                   jax.ShapeDtypeStruct((T, 2 * D_REC), F32),
                   jax.ShapeDtypeStruct((T, 2 * D), F32),
                   jax.ShapeDtypeStruct((T, D), BF16)),
        grid=(T // tm, N_DZ_TILES),
        in_specs=[pl.BlockSpec((tm, D), lambda i, j: (i, 0)),
                  pl.BlockSpec((1, D), lambda i, j: (0, 0)),
                  pl.BlockSpec((TILE, D), lambda i, j: (j, 0)),
                  pl.BlockSpec((1, TILE), lambda i, j: (0, j))],
        out_specs=(pl.BlockSpec((tm, TILE), lambda i, j: (i, jnp.minimum(j, 2))),
                   pl.BlockSpec((tm, TILE), lambda i, j: (i, jnp.clip(j - 3, 0, 3))),
                   pl.BlockSpec((tm, TILE), lambda i, j: (i, jnp.clip(j - 7, 0, 3))),
                   pl.BlockSpec((tm, D), lambda i, j: (i, 0))),
        scratch_shapes=[pltpu.VMEM((tm, D), BF16)],
        compiler_params=_params(dimension_semantics=("parallel", "arbitrary")),
    )(x, g1, w_in_t, b_in)


def _dz_specs(rows, tile_of, row_of):
    def spec(off, n, per_plane):
        def index(*ids):
            t = jnp.clip(tile_of(*ids) - off, 0, n - 1)
            return (t // per_plane, row_of(*ids), t % per_plane)
        return pl.BlockSpec((1, rows, TILE), index)
    return [spec(off, n, per) for off, n, per in DZ_ARRAYS]


def _dh_norm1_bwd(dz, w_in_t, x, g1, dx1):
    tm = 1024

    def body(*refs):
        seg_refs = refs[:3]
        w_ref, x_ref, g_ref, dx1_ref, gx_ref, dg_ref, acc = refs[3:]
        i, kk = pl.program_id(0), pl.program_id(1)

        @pl.when(kk == 0)
        def _():
            acc[...] = jnp.zeros_like(acc)

        for s, (off, n, _) in enumerate(DZ_ARRAYS):
            @pl.when((kk >= off) & (kk < off + n))
            def _(s=s):
                acc[...] += _dot(seg_refs[s][0], w_ref[...])

        @pl.when((i == 0) & (kk == 0))
        def _():
            dg_ref[...] = jnp.zeros_like(dg_ref)

        @pl.when(kk == N_DZ_TILES - 1)
        def _():
            xv = x_ref[...]
            r = lax.rsqrt(jnp.mean(xv * xv, axis=-1, keepdims=True) + EPS)
            xh = xv * r
            dh = acc[...]
            dg_ref[...] += jnp.sum(dh * xh, axis=0, keepdims=True)
            gx_ref[...] = dx1_ref[...] + _rms_bwd(dh, xh, r, g_ref[...])

    tok = pl.BlockSpec((tm, D), lambda i, j: (i, 0))
    vec = pl.BlockSpec((1, D), lambda i, j: (0, 0))
    return pl.pallas_call(
        body, name="dh_norm1_bwd",
        out_shape=(jax.ShapeDtypeStruct((T, D), F32), jax.ShapeDtypeStruct((1, D), F32)),
        grid=(T // tm, N_DZ_TILES),
        in_specs=_dz_specs(tm, lambda i, j: j, lambda i, j: i)
        + [pl.BlockSpec((TILE, D), lambda i, j: (j, 0)), tok, vec, tok],
        out_specs=(tok, vec),
        scratch_shapes=[pltpu.VMEM((tm, D), F32)],
        compiler_params=_params(dimension_semantics=("arbitrary", "arbitrary")),
    )(*dz, w_in_t, x, g1, dx1)


def _grad_w_in(dz, h):
    def body(*refs):
        seg_refs = refs[:3]
        h_ref, gw_ref, gb_ref = refs[3:]
        j = pl.program_id(0)

        for s, (off, n, _) in enumerate(DZ_ARRAYS):
            @pl.when((j >= off) & (j < off + n))
            def _(s=s):
                a = seg_refs[s][0]
                gw_ref[...] = _dot_tn(a, h_ref[...]).astype(BF16)
                gb_ref[...] = jnp.sum(a.astype(F32), axis=0, keepdims=True)

    return pl.pallas_call(
        body, name="grad_w_in",
        out_shape=(jax.ShapeDtypeStruct((D_IN, D), BF16), jax.ShapeDtypeStruct((1, D_IN), F32)),
        grid=(N_DZ_TILES,),
        in_specs=_dz_specs(T, lambda j: j, lambda j: 0) + [pl.BlockSpec((T, D), lambda j: (0, 0))],
        out_specs=(pl.BlockSpec((TILE, D), lambda j: (j, 0)), pl.BlockSpec((1, TILE), lambda j: (0, j))),
        compiler_params=_params(dimension_semantics=("parallel",)),
    )(*dz, h)


def _rpb_rows(rpb):
    padded = jnp.pad(rpb, ((0, 0), (0, 0), (0, GRID_W - N_RPB_C)))
    rows = [padded[:, WIN_H - 1 - oi: 2 * WIN_H - 1 - oi].reshape(N_HEADS // 2, 2, KWIN)
            for oi in range(WIN_H)]
    return jnp.stack(rows, axis=0)


SKEW = KWIN - (WIN_W - 1)


MASKED = -1e30


def _bias_tiles(rows_ref, valid, bias_s):
    for oi in range(WIN_H):
        for hh in range(2):
            row = jnp.broadcast_to(rows_ref[oi, 0, hh:hh + 1, :], (GRID_W, KWIN))
            tile = pltpu.roll(row, SKEW, 1, stride=1, stride_axis=0)
            bias_s[oi, hh * GRID_W:(hh + 1) * GRID_W, :] = jnp.where(valid[:GRID_W], tile, MASKED)


def _bias_tile_grads(gb_s, flip, out_ref):
    for oi in range(WIN_H):
        for hh in range(2):
            g = _dot_exact(flip, gb_s[oi, hh * GRID_W:(hh + 1) * GRID_W, :])
            back = pltpu.roll(g, KWIN - (GRID_W - WIN_W), 1, stride=1, stride_axis=0)
            out_ref[0, oi, hh:hh + 1, :] = jnp.sum(back, axis=0, keepdims=True)


def _rpb_fold(row_grads):
    g = row_grads.transpose(1, 0, 2, 3).reshape(WIN_H, N_HEADS, WIN_H, GRID_W)
    g = g.transpose(0, 2, 1, 3)

    def body(g_ref, o_ref):
        for dr in range(N_RPB_R):
            terms = [g_ref[oi, i] for oi in range(WIN_H) for i in range(WIN_H) if i - oi + WIN_H - 1 == dr]
            acc = terms[0]
            for term in terms[1:]:
                acc = acc + term
            o_ref[dr] = acc

    out = pl.pallas_call(
        body, name="rpb_fold",
        out_shape=jax.ShapeDtypeStruct((N_RPB_R, N_HEADS, GRID_W), F32),
    )(g)
    return out.transpose(1, 0, 2)[:, :, :N_RPB_C]


def _att_scores(q_ref, k_ref, bias_ref, hmask, r):
    rs = jnp.clip(r - WIN_H // 2, 0, ROWS - WIN_H)
    oi = r - rs
    q0 = pl.multiple_of(r * GRID_W, GRID_W)
    k0 = pl.multiple_of(rs * GRID_W, GRID_W)
    q_r = q_ref[pl.ds(q0, GRID_W), :] * (DH ** -0.5)
    q2 = jnp.where(hmask, jnp.concatenate([q_r, q_r], axis=0), jnp.zeros((), BF16))
    kw = k_ref[pl.ds(k0, KWIN), :]
    s = _dot_nt(q2, kw) + bias_ref[oi]
    e = jnp.exp(s - jnp.max(s, axis=-1, keepdims=True))
    return e, 1.0 / jnp.sum(e, axis=-1, keepdims=True), q2, kw, q0, k0, oi


def _att_fwd(qkv, bias_rows):
    valid_np, hmask_np = _att_tables()

    def body(q_ref, k_ref, v_ref, rows_ref, valid_ref, hmask_ref, o_ref, bias_s):
        valid = valid_ref[...] > 0.5
        hmask = hmask_ref[...] > 0.5
        first_head = lax.broadcasted_iota(jnp.int32, (GRID_W, 2 * DH), 1) < DH
        _bias_tiles(rows_ref, valid, bias_s)

        def row(r, carry):
            e, rl, _, _, q0, k0, _ = _att_scores(q_ref, k_ref, bias_s, hmask, r)
            o2 = _dot((e * rl).astype(BF16), v_ref[pl.ds(k0, KWIN), :])
            o_ref[pl.ds(q0, GRID_W), :] = jnp.where(first_head, o2[:GRID_W], o2[GRID_W:]).astype(BF16)
            return carry

        lax.fori_loop(0, ROWS, row, 0, unroll=4)

    col = lambda off: pl.BlockSpec((T, 2 * DH), lambda hp: (0, hp + off))
    return pl.pallas_call(
        body, name="att_fwd",
        out_shape=jax.ShapeDtypeStruct((T, D_ATT), BF16),
        grid=(N_HEADS // 2,),
        in_specs=[col(0), col(4), col(8),
                  pl.BlockSpec((WIN_H, 1, 2, KWIN), lambda hp: (0, hp, 0, 0)),
                  pl.BlockSpec((2 * GRID_W, KWIN), lambda hp: (0, 0)),
                  pl.BlockSpec((2 * DH, 2 * DH), lambda hp: (0, 0))],
        out_specs=pl.BlockSpec((T, 2 * DH), lambda hp: (0, hp)),
        scratch_shapes=[pltpu.VMEM((WIN_H, 2 * GRID_W, KWIN), F32)],
        compiler_params=_params(dimension_semantics=("parallel",)),
    )(qkv, qkv, qkv, bias_rows, jnp.asarray(valid_np), jnp.asarray(hmask_np))


def _att_bwd(qkv, bias_rows, datt, after):
    valid_np, hmask_np = _att_tables()

    def body(q_ref, k_ref, v_ref, do_ref, rows_ref, valid_ref, hmask_ref, flip_ref,
             dqkv_ref, grows_ref, dk_acc, dv_acc, bias_s, gb_s):
        valid = valid_ref[...] > 0.5
        hmask = hmask_ref[...] > 0.5
        first_head = lax.broadcasted_iota(jnp.int32, (GRID_W, 2 * DH), 1) < DH
        dk_acc[...] = jnp.zeros_like(dk_acc)
        dv_acc[...] = jnp.zeros_like(dv_acc)
        gb_s[...] = jnp.zeros_like(gb_s)
        _bias_tiles(rows_ref, valid, bias_s)

        def row(r, carry):
            e, rl, q2, kw, q0, k0, oi = _att_scores(q_ref, k_ref, bias_s, hmask, r)
            do_r = do_ref[pl.ds(q0, GRID_W), :]
            do2 = jnp.where(hmask, jnp.concatenate([do_r, do_r], axis=0), jnp.zeros((), BF16))
            vw = v_ref[pl.ds(k0, KWIN), :]
            p = e * rl
            dp = _dot_nt(do2, vw)
            ds = p * (dp - jnp.sum(dp * p, axis=-1, keepdims=True))
            p16 = p.astype(BF16)
            ds16 = ds.astype(BF16)
            dv_acc[pl.ds(k0, KWIN), :] += _dot_tn(p16, do2)
            dk_acc[pl.ds(k0, KWIN), :] += _dot_tn(ds16, q2)
            dq2 = _dot(ds16, kw) * (DH ** -0.5)
            dqkv_ref[0, pl.ds(q0, GRID_W), :] = jnp.where(first_head, dq2[:GRID_W], dq2[GRID_W:]).astype(BF16)
            gb_s[oi] += ds
            return carry

        lax.fori_loop(0, ROWS, row, 0, unroll=4)
        dqkv_ref[1] = dk_acc[...].astype(BF16)
        dqkv_ref[2] = dv_acc[...].astype(BF16)
        _bias_tile_grads(gb_s, flip_ref[...], grows_ref)

    col = lambda off: pl.BlockSpec((T, 2 * DH), lambda hp: (0, hp + off))
    tiles = pltpu.VMEM((WIN_H, 2 * GRID_W, KWIN), F32)
    return pl.pallas_call(
        body, name="att_bwd",
        out_shape=(jax.ShapeDtypeStruct((3, T, D_ATT), BF16),
                   jax.ShapeDtypeStruct((N_HEADS // 2, WIN_H, 2, KWIN), F32)),
        grid=(N_HEADS // 2,),
        in_specs=[col(0), col(4), col(8), col(0),
                  pl.BlockSpec((WIN_H, 1, 2, KWIN), lambda hp: (0, hp, 0, 0)),
                  pl.BlockSpec((2 * GRID_W, KWIN), lambda hp: (0, 0)),
                  pl.BlockSpec((2 * DH, 2 * DH), lambda hp: (0, 0)),
                  pl.BlockSpec((GRID_W, GRID_W), lambda hp: (0, 0))],
        out_specs=(pl.BlockSpec((3, T, 2 * DH), lambda hp: (0, 0, hp)),
                   pl.BlockSpec((1, WIN_H, 2, KWIN), lambda hp: (hp, 0, 0, 0))),
        scratch_shapes=[pltpu.VMEM((T, 2 * DH), F32), pltpu.VMEM((T, 2 * DH), F32), tiles, tiles],
        compiler_params=_params(dimension_semantics=("parallel",)),
    )(qkv, qkv, qkv, datt, bias_rows, jnp.asarray(valid_np) + after, jnp.asarray(hmask_np),
      jnp.asarray(np.eye(GRID_W, dtype=np.float32)[::-1].copy()))


def _conv_taps(up):
    return (_shift_rows(up, 2), _shift_rows(up, 1), up, _shift_rows(up, -1))


def _pair_block_diag(w_pair, dup, same_half):
    return jnp.where(same_half, _dot(w_pair.astype(BF16), dup), 0.0).astype(BF16)


def _gates(u, u16, wa, ba, wi, bi, lam):
    r = _sigmoid(_dot(u16, wa) + ba)
    ig = _sigmoid(_dot(u16, wi) + bi)
    sp = _softplus(-lam)
    log_a = (-LRU_C) * r * sp
    a = jnp.exp(log_a)
    mult2 = jnp.maximum(_one_minus_square(log_a, a), 0.0)
    return r, ig, sp, a, jnp.sqrt(mult2), mult2


SCAN_BLOCKS = 2


def _scans(jobs):
    c = jobs[0][0].shape[1]
    nblk = T // 8
    rows = lax.broadcasted_iota(jnp.int32, (8, c), 0)

    def block(a, b, reverse):
        for s in (1, 2, 4):
            if reverse:
                keep = rows < 8 - s
                a_s = jnp.where(keep, pltpu.roll(a, 8 - s, 0), 1.0)
                b_s = jnp.where(keep, pltpu.roll(b, 8 - s, 0), 0.0)
            else:
                keep = rows >= s
                a_s = jnp.where(keep, pltpu.roll(a, s, 0), 1.0)
                b_s = jnp.where(keep, pltpu.roll(b, s, 0), 0.0)
            b = a * b_s + b
            a = a * a_s
        return a, b

    def step(i, carry):
        out = []
        for (a_ref, b_ref, h_ref, reverse), h_prev in zip(jobs, carry):
            for u in range(SCAN_BLOCKS):
                blk = i * SCAN_BLOCKS + u
                if reverse:
                    blk = nblk - 1 - blk
                t0 = pl.multiple_of(blk * 8, 8)
                a, b = block(a_ref[pl.ds(t0, 8), :], b_ref[pl.ds(t0, 8), :], reverse)
                h = a * h_prev + b
                h_ref[pl.ds(t0, 8), :] = h
                h_prev = jnp.broadcast_to(h[0:1] if reverse else h[7:8], (8, c))
            out.append(h_prev)
        return tuple(out)

    lax.fori_loop(0, nblk // SCAN_BLOCKS, step, tuple(jnp.zeros((8, c), F32) for _ in jobs))


def _rec_specs():
    tok = lambda off: pl.BlockSpec((T, CG), lambda g: (0, g + off))
    per_ch = lambda rows: pl.BlockSpec((rows, CG), lambda g: (0, g))
    wspec = pl.BlockSpec((2, 1, CG, REC_BLOCK), lambda g: (0, g, 0, 0))
    const = lambda shape: pl.BlockSpec(shape, lambda g: (0, 0))
    return tok, per_ch, wspec, const


def _rec_fwd(uy, conv_w, conv_b, w_a, b_a, w_i, b_i, lam):
    tok, per_ch, wspec, const = _rec_specs()

    def body(up_ref, yb_ref, cw_ref, cb_ref, wa_ref, ba_ref, wi_ref, bi_ref, lam_ref, dup_ref, half_ref,
             hf_ref, hb_ref, yrec_ref, a_f, bx_f, a_b, bx_b):
        dup = dup_ref[...]
        same_half = half_ref[...] > 0.5
        taps = _conv_taps(up_ref[...])
        u = cb_ref[...]
        for j in range(4):
            u = u + taps[j] * cw_ref[j:j + 1, :]
        u16 = u.astype(BF16)
        for d, (a_s, bx_s) in enumerate(((a_f, bx_f), (a_b, bx_b))):
            wa = _pair_block_diag(wa_ref[d, 0], dup, same_half)
            wi = _pair_block_diag(wi_ref[d, 0], dup, same_half)
            _, ig, _, a, mult, _ = _gates(u, u16, wa, ba_ref[d:d + 1, :], wi, bi_ref[d:d + 1, :],
                                       lam_ref[d:d + 1, :])
            a_s[...] = a
            bx_s[...] = mult * (ig * u)
        _scans([(a_f, bx_f, hf_ref, False), (a_b, bx_b, hb_ref, True)])
        gelu, _ = _gelu_and_grad(yb_ref[...])
        yrec_ref[...] = ((hf_ref[...] + hb_ref[...]) * gelu).astype(BF16)

    return pl.pallas_call(
        body, name="rec_fwd",
        out_shape=(jax.ShapeDtypeStruct((T, D_REC), F32), jax.ShapeDtypeStruct((T, D_REC), F32),
                   jax.ShapeDtypeStruct((T, D_REC), BF16)),
        grid=(N_CG,),
        in_specs=[tok(0), tok(N_CG), per_ch(4), per_ch(1), wspec, per_ch(2), wspec, per_ch(2), per_ch(2),
                  const((REC_BLOCK, CG)), const((CG, CG))],
        out_specs=(tok(0), tok(0), tok(0)),
        scratch_shapes=[pltpu.VMEM((T, CG), F32)] * 4,
        compiler_params=_params(dimension_semantics=("parallel",)),
    )(uy, uy, conv_w, conv_b, w_a, b_a, w_i, b_i, lam,
      jnp.asarray(_dup_table(), BF16), jnp.asarray(_pair_mask()))


def _rec_bwd(uy, hf, hb, dyrec, conv_w, conv_b, w_a, b_a, w_i, b_i, lam):
    tok, per_ch, wspec, const = _rec_specs()

    def body(up_ref, yb_ref, hf_ref, hb_ref, dy_ref, cw_ref, cb_ref, wa_ref, ba_ref, wi_ref, bi_ref,
             lam_ref, dup_ref, dupt_ref, half_ref,
             duy_ref, dcw_ref, dcb_ref, dwa_ref, dba_ref, dwi_ref, dbi_ref, dlam_ref,
             a_s0, a_s1, dh_s, g_s0, g_s1):
        dup = dup_ref[...]
        dup_t = dupt_ref[...]
        same_half = half_ref[...] > 0.5
        taps = _conv_taps(up_ref[...])
        u = cb_ref[...]
        for j in range(4):
            u = u + taps[j] * cw_ref[j:j + 1, :]
        u16 = u.astype(BF16)
        gelu, dgelu = _gelu_and_grad(yb_ref[...])
        dy = dy_ref[...]
        duy_ref[1] = (dy * (hf_ref[...] + hb_ref[...]) * dgelu).astype(BF16)
        dh_s[...] = dy * gelu
        gate_values = []
        for d, a_s in enumerate((a_s0, a_s1)):
            wa = _pair_block_diag(wa_ref[d, 0], dup, same_half)
            wi = _pair_block_diag(wi_ref[d, 0], dup, same_half)
            lam_d = lam_ref[d:d + 1, :]
            r, ig, sp, a, mult, mult2 = _gates(u, u16, wa, ba_ref[d:d + 1, :], wi, bi_ref[d:d + 1, :], lam_d)
            a_s[...] = _shift_rows(a, 1 if d == 1 else -1)
            gate_values.append((wa, wi, lam_d, r, ig, sp, a, mult, mult2))
        _scans([(a_s0, dh_s, g_s0, True), (a_s1, dh_s, g_s1, False)])
        du = jnp.zeros((T, CG), F32)
        for d, g_s in enumerate((g_s0, g_s1)):
            reverse = d == 1
            wa, wi, lam_d, r, ig, sp, a, mult, mult2 = gate_values[d]
            g = g_s[...]
            h_prev = _shift_rows(hb_ref[...], -1) if reverse else _shift_rows(hf_ref[...], 1)
            da = g * h_prev
            dmult = g * (ig * u)
            dig = g * mult * u
            du = du + g * mult * ig
            dmult_dlog = jnp.where(mult2 > 0.0, -(a * a) * lax.rsqrt(mult2), 0.0)
            dlog_a = da * a + dmult * dmult_dlog
            dr = dlog_a * ((-LRU_C) * sp)
            dsp = jnp.sum(dlog_a * ((-LRU_C) * r), axis=0, keepdims=True)
            dlam_ref[d:d + 1, :] = dsp * (-_sigmoid(-lam_d))
            dga = dr * r * (1.0 - r)
            dgi = dig * ig * (1.0 - ig)
            dga16 = dga.astype(BF16)
            dgi16 = dgi.astype(BF16)
            du = du + _dot_nt(dga16, wa) + _dot_nt(dgi16, wi)
            dwa_ref[d, 0] = _dot_exact(jnp.where(same_half, _dot_tn(u16, dga16), 0.0), dup_t)
            dwi_ref[d, 0] = _dot_exact(jnp.where(same_half, _dot_tn(u16, dgi16), 0.0), dup_t)
            dba_ref[d:d + 1, :] = jnp.sum(dga, axis=0, keepdims=True)
            dbi_ref[d:d + 1, :] = jnp.sum(dgi, axis=0, keepdims=True)
        dcb_ref[...] = jnp.sum(du, axis=0, keepdims=True)
        for j in range(4):
            dcw_ref[j:j + 1, :] = jnp.sum(du * taps[j], axis=0, keepdims=True)
        dup_in = (_shift_rows(du, -2) * cw_ref[0:1, :] + _shift_rows(du, -1) * cw_ref[1:2, :]
                  + du * cw_ref[2:3, :] + _shift_rows(du, 1) * cw_ref[3:4, :])
        duy_ref[0] = dup_in.astype(BF16)

    wshape = jax.ShapeDtypeStruct((2, N_CG, CG, REC_BLOCK), F32)
    vec = lambda rows: jax.ShapeDtypeStruct((rows, D_REC), F32)
    dup_np = _dup_table()
    return pl.pallas_call(
        body, name="rec_bwd",
        out_shape=(jax.ShapeDtypeStruct((2, T, D_REC), BF16),
                   vec(4), vec(1), wshape, vec(2), wshape, vec(2), vec(2)),
        grid=(N_CG,),
        in_specs=[tok(0), tok(N_CG), tok(0), tok(0), tok(0),
                  per_ch(4), per_ch(1), wspec, per_ch(2), wspec, per_ch(2), per_ch(2),
                  const((REC_BLOCK, CG)), const((CG, REC_BLOCK)), const((CG, CG))],
        out_specs=(pl.BlockSpec((2, T, CG), lambda g: (0, 0, g)),
                   per_ch(4), per_ch(1), wspec, per_ch(2), wspec, per_ch(2), per_ch(2)),
        scratch_shapes=[pltpu.VMEM((T, CG), F32)] * 5,
        compiler_params=_params(dimension_semantics=("parallel",)),
    )(uy, uy, hf, hb, dyrec, conv_w, conv_b, w_a, b_a, w_i, b_i, lam,
      jnp.asarray(dup_np, BF16), jnp.asarray(dup_np.T.copy()), jnp.asarray(_pair_mask()))


TM_MIX = 256


def _mix_specs():
    tok = lambda width, blk=0: pl.BlockSpec((TM_MIX, width), lambda i: (i, blk))
    full = lambda shape: pl.BlockSpec(shape, lambda i: (0, 0))
    return tok, full


def _mix_fwd(x, att, yrec, gg, w_att_o_t, w_rec_o, w_out):
    tok, full = _mix_specs()

    def body(x_ref, att_ref, yr_ref, ga_ref, gr_ref, wao_ref, wro_ref, wo_ref, x1_ref, mixed_ref):
        y_att = _dot_nt(att_ref[...], wao_ref[...])
        y_rec = _dot(yr_ref[...], wro_ref[...])
        mixed = (_sigmoid(ga_ref[...]) * y_att + _sigmoid(gr_ref[...]) * y_rec).astype(BF16)
        mixed_ref[...] = mixed
        x1_ref[...] = x_ref[...] + _dot(mixed, wo_ref[...])

    return pl.pallas_call(
        body, name="mix_fwd",
        out_shape=(jax.ShapeDtypeStruct((T, D), F32), jax.ShapeDtypeStruct((T, D), BF16)),
        grid=(T // TM_MIX,),
        in_specs=[tok(D), tok(D_ATT), tok(D_REC), tok(D, 0), tok(D, 1),
                  full((D, D_ATT)), full((D_REC, D)), full((D, D))],
        out_specs=(tok(D), tok(D)),
        compiler_params=_params(dimension_semantics=("parallel",)),
    )(x, att, yrec, gg, gg, w_att_o_t, w_rec_o, w_out)


def _mix_bwd(dx1, att, yrec, gg, w_att_o_t, w_rec_o, w_out):
    tok, full = _mix_specs()

    def body(dx_ref, att_ref, yr_ref, ga_ref, gr_ref, wao_ref, wro_ref, wo_ref,
             dgg_ref, dya_ref, dyr_ref, datt_ref, dyrp_ref):
        dmixed = _dot_nt(dx_ref[...].astype(BF16), wo_ref[...])
        y_att = _dot_nt(att_ref[...], wao_ref[...])
        y_rec = _dot(yr_ref[...], wro_ref[...])
        sa = _sigmoid(ga_ref[...])
        sr = _sigmoid(gr_ref[...])
        dgg_ref[0] = (dmixed * y_att * sa * (1.0 - sa)).astype(BF16)
        dgg_ref[1] = (dmixed * y_rec * sr * (1.0 - sr)).astype(BF16)
        dya = (dmixed * sa).astype(BF16)
        dyr = (dmixed * sr).astype(BF16)
        dya_ref[...] = dya
        dyr_ref[...] = dyr
        datt_ref[...] = _dot(dya, wao_ref[...]).astype(BF16)
        dyrp_ref[...] = _dot_nt(dyr, wro_ref[...])

    return pl.pallas_call(
        body, name="mix_bwd",
        out_shape=(jax.ShapeDtypeStruct((2, T, D), BF16),
                   jax.ShapeDtypeStruct((T, D), BF16), jax.ShapeDtypeStruct((T, D), BF16),
                   jax.ShapeDtypeStruct((T, D_ATT), BF16), jax.ShapeDtypeStruct((T, D_REC), F32)),
        grid=(T // TM_MIX,),
        in_specs=[tok(D), tok(D_ATT), tok(D_REC), tok(D, 0), tok(D, 1),
                  full((D, D_ATT)), full((D_REC, D)), full((D, D))],
        out_specs=(pl.BlockSpec((2, TM_MIX, D), lambda i: (0, i, 0)),
                   tok(D), tok(D), tok(D_ATT), tok(D_REC)),
        compiler_params=_params(dimension_semantics=("parallel",)),
    )(dx1, att, yrec, gg, gg, w_att_o_t, w_rec_o, w_out)


TM_FFN = 256
FF_CHUNK = 1024


def _ffn_loss(x1, target, g2, gf, w_ff1_t, w_ff2):
    n_chunks = D_FF // FF_CHUNK

    def body(x1_ref, tg_ref, g2_ref, gf_ref, w1_hbm, w2_hbm,
             loss_ref, dx1_ref, h2_ref, act_ref, dpre_ref, dx2_ref, dg2_ref, dgf_ref,
             w1, w2, relu_s):
        i = pl.program_id(0)

        @pl.when(i == 0)
        def _():
            pltpu.sync_copy(w1_hbm, w1)
            pltpu.sync_copy(w2_hbm, w2)
            loss_ref[...] = jnp.zeros_like(loss_ref)
            dg2_ref[...] = jnp.zeros_like(dg2_ref)
            dgf_ref[...] = jnp.zeros_like(dgf_ref)

        x1v = x1_ref[...]
        r2 = lax.rsqrt(jnp.mean(x1v * x1v, axis=-1, keepdims=True) + EPS)
        xh2 = x1v * r2
        h2 = (xh2 * g2_ref[...]).astype(BF16)
        h2_ref[...] = h2
        x2 = x1v
        for c in range(n_chunks):
            ff = slice(c * FF_CHUNK, (c + 1) * FF_CHUNK)
            rl = jnp.maximum(_dot_nt(h2, w1[ff, :]), 0.0)
            relu_s[:, ff] = rl
            act = (rl * rl).astype(BF16)
            act_ref[:, ff] = act
            x2 = x2 + _dot(act, w2[ff, :])
        r3 = lax.rsqrt(jnp.mean(x2 * x2, axis=-1, keepdims=True) + EPS)
        xh3 = x2 * r3
        err = xh3 * gf_ref[...] - tg_ref[...]
        loss_ref[...] += 0.5 * jnp.sum(jnp.mean(err * err, axis=-1, keepdims=True))
        dy = err * (1.0 / D)
        dgf_ref[...] += jnp.sum(dy * xh3, axis=0, keepdims=True)
        dx2 = _rms_bwd(dy, xh3, r3, gf_ref[...])
        dx2_16 = dx2.astype(BF16)
        dx2_ref[...] = dx2_16
        dh2 = jnp.zeros((TM_FFN, D), F32)
        for c in range(n_chunks):
            ff = slice(c * FF_CHUNK, (c + 1) * FF_CHUNK)
            dpre = (_dot_nt(dx2_16, w2[ff, :]) * (2.0 * relu_s[:, ff])).astype(BF16)
            dpre_ref[:, ff] = dpre
            dh2 = dh2 + _dot(dpre, w1[ff, :])
        dg2_ref[...] += jnp.sum(dh2 * xh2, axis=0, keepdims=True)
        dx1_ref[...] = dx2 + _rms_bwd(dh2, xh2, r2, g2_ref[...])

    tok = lambda width: pl.BlockSpec((TM_FFN, width), lambda i: (i, 0))
    vec = pl.BlockSpec((1, D), lambda i: (0, 0))
    hbm = pl.BlockSpec(memory_space=pl.ANY)
    return pl.pallas_call(
        body, name="ffn_loss",
        out_shape=(jax.ShapeDtypeStruct((8, 128), F32), jax.ShapeDtypeStruct((T, D), F32),
                   jax.ShapeDtypeStruct((T, D), BF16), jax.ShapeDtypeStruct((T, D_FF), BF16),
                   jax.ShapeDtypeStruct((T, D_FF), BF16), jax.ShapeDtypeStruct((T, D), BF16),
                   jax.ShapeDtypeStruct((1, D), F32), jax.ShapeDtypeStruct((1, D), F32)),
        grid=(T // TM_FFN,),
        in_specs=[tok(D), tok(D), vec, vec, hbm, hbm],
        out_specs=(pl.BlockSpec((8, 128), lambda i: (0, 0)), tok(D), tok(D), tok(D_FF), tok(D_FF), tok(D),
                   vec, vec),
        scratch_shapes=[pltpu.VMEM((D_FF, D), BF16), pltpu.VMEM((D_FF, D), BF16),
                        pltpu.VMEM((TM_FFN, D_FF), F32)],
        compiler_params=_params(dimension_semantics=("arbitrary",)),
    )(x1, target, g2, gf, w_ff1_t, w_ff2)


def _local_step(x, target, p, late_weights, reduce_early):
    bias = _rpb_rows(p["rpb"])
    pairs = lambda w: w.reshape(2, N_CG, CG, REC_BLOCK)
    w_a, w_i = pairs(p["w_rg_a"]), pairs(p["w_rg_i"])
    rec_params = (p["conv_w"], p["conv_b"], w_a, p["b_rg_a"], w_i, p["b_rg_i"], p["lru_lambda"])

    qkv, uy, gg, h = _in_proj(x, p["ln1_g"], p["w_in_t"], p["b_in"])
    att = _att_fwd(qkv, bias)
    hf, hb, yrec = _rec_fwd(uy, *rec_params)
    p = {**p, **late_weights(yrec)}
    x1, mixed = _mix_fwd(x, att, yrec, gg, p["w_att_o_t"], p["w_rec_o"], p["w_out"])
    loss8, dx1, h2, act, dpre, dx2, g_ln2, g_lnf = _ffn_loss(
        x1, target, p["ln2_g"], p["lnf_g"], p["w_ff1_t"], p["w_ff2"])

    dgg, dya, dyr, datt, dyrp = _mix_bwd(dx1, att, yrec, gg, p["w_att_o_t"], p["w_rec_o"], p["w_out"])
    duy, g_cw, g_cb, g_wa, g_ba, g_wi, g_bi, g_lam = _rec_bwd(uy, hf, hb, dyrp, *rec_params)
    blocks = lambda g: g.reshape(2, N_REC_BLOCKS, REC_BLOCK, REC_BLOCK)
    grads = {
        "w_att_o_t": _matmul(dya, att, "tn", BF16, "g_w_att_o"),
        "conv_w": g_cw, "conv_b": g_cb, "w_rg_a": blocks(g_wa), "b_rg_a": g_ba,
        "w_rg_i": blocks(g_wi), "b_rg_i": g_bi, "lru_lambda": g_lam,
        "w_rec_o": _matmul(yrec, dyr, "tn", BF16, "g_w_rec_o"),
        "w_out": _matmul(mixed, dx1, "tn", BF16, "g_w_out"),
        "ln2_g": g_ln2,
        "w_ff1_t": _matmul(dpre, h2, "tn", BF16, "g_w_ff1"),
        "w_ff2": _matmul(act, dx2, "tn", BF16, "g_w_ff2"),
        "lnf_g": g_lnf,
    }
    after = reduce_early(grads)
    dqkv, gbias = _att_bwd(qkv, bias, datt, after)
    dz = (dqkv, duy, dgg)
    grad_x, g_ln1 = _dh_norm1_bwd(dz, p["w_in_t"], x, p["ln1_g"], dx1)
    g_w_in_t, g_b_in = _grad_w_in(dz, h)
    grads.update(ln1_g=g_ln1, w_in_t=g_w_in_t, b_in=g_b_in, rpb=_rpb_fold(gbias))
    return loss8[0:1, 0:1], grad_x, grads


MESH_ID = pl.DeviceIdType.MESH
ANY = pl.BlockSpec(memory_space=pl.ANY)

CHAN_BLOCK_ROWS = 32
GATE_ROWS = 2 * 2 * N_REC_BLOCKS * REC_BLOCK * REC_BLOCK // (N_DEV * D)
SECTIONS = (("w_in_t", 704, D), ("w_rec_o", 128, D), ("w_out", 128, D), ("w_ff1_t", 512, D),
            ("w_ff2", 512, D), ("chan", CHAN_BLOCK_ROWS, D), ("w_att_o_t", 128, D_ATT),
            ("gates", GATE_ROWS, D))
N_SEC = len(SECTIONS)
N_CHAN_ROWS = 10
CHAN = (("conv_w", 4), ("b_rg_a", 2), ("b_rg_i", 2), ("lru_lambda", 2))


def _position():
    return lax.axis_index("x"), lax.axis_index("y"), lax.axis_index("c")


def _other_chips(x, y):
    return [(1 - x, y), (x, 1 - y), (1 - x, 1 - y)]


def _block_of(ref, dev, rows):
    return ref.at[pl.ds(pl.multiple_of(dev * rows, 16), rows)]


def _all_gather(shards, name):
    ns = len(shards)

    def body(*refs):
        x_refs, out_refs, done_ref = refs[:ns], refs[ns:2 * ns], refs[2 * ns]
        send_sems, recv_sems, local_sems = refs[2 * ns + 1:]
        done_ref[0, 0] = 0.0
        x, y, c = _position()
        me, sibling = (x, y, c), (x, y, 1 - c)
        x_nbr, y_nbr, diagonal = _other_chips(x, y)
        north = c == 1
        relay_from = (jnp.where(north, x_nbr[0], y_nbr[0]), jnp.where(north, x_nbr[1], y_nbr[1]))
        relay_to = (jnp.where(north, y_nbr[0], x_nbr[0]), jnp.where(north, y_nbr[1], x_nbr[1]))

        def rows(s, px, py, pc):
            return _block_of(out_refs[s], 4 * px + 2 * py + pc, shards[s].shape[0])

        def copy(k, s, block, to, from_shard=False):
            return pltpu.make_async_remote_copy(
                src_ref=x_refs[s] if from_shard else rows(s, *block), dst_ref=rows(s, *block),
                send_sem=send_sems.at[k * ns + s], recv_sem=recv_sems.at[k * ns + s],
                device_id=to, device_id_type=MESH_ID)

        sections = range(ns)
        mine = [pltpu.make_async_copy(x_refs[s], rows(s, *me), local_sems.at[s]) for s in sections]
        sent = [copy(k, s, me, to, True) for k, to in enumerate((sibling, (*x_nbr, c), (*y_nbr, c)))
                for s in sections]
        for cp in mine + sent:
            cp.start()
        for s in sections:
            copy(1, s, (*x_nbr, c), me).wait_recv()
            copy(2, s, (*y_nbr, c), me).wait_recv()
            sent += [copy(3, s, (*relay_from, c), (*relay_to, c)),
                     copy(4, s, (*x_nbr, c), sibling), copy(5, s, (*y_nbr, c), sibling)]
            for cp in sent[-3:]:
                cp.start()
        for s in sections:
            copy(3, s, (*diagonal, c), me).wait_recv()
            sent.append(copy(6, s, (*diagonal, c), sibling))
            sent[-1].start()
        for s in sections:
            copy(0, s, sibling, me).wait_recv()
            for k, chip in ((4, x_nbr), (5, y_nbr), (6, diagonal)):
                copy(k, s, (*chip, 1 - c), me).wait_recv()
        for cp in sent:
            cp.wait_send()
        for cp in mine:
            cp.wait()

    return pl.pallas_call(
        body, name=name,
        out_shape=tuple(jax.ShapeDtypeStruct((N_DEV * s.shape[0], s.shape[1]), s.dtype) for s in shards)
        + (jax.ShapeDtypeStruct((1, 1), F32),),
        in_specs=[ANY] * ns,
        out_specs=(ANY,) * ns + (pl.BlockSpec(memory_space=pltpu.SMEM),),
        scratch_shapes=[pltpu.SemaphoreType.DMA((7 * ns,)), pltpu.SemaphoreType.DMA((7 * ns,)),
                        pltpu.SemaphoreType.DMA((ns,))],
    )(*shards)


HBM = pl.BlockSpec(memory_space=pltpu.HBM)
SEM = pl.BlockSpec(memory_space=pltpu.SEMAPHORE)
EFFECT = pltpu.SideEffectType.DATAFLOW_SIDE_EFFECTING


def _in_hbm(a):
    return pltpu.with_memory_space_constraint(a, pltpu.HBM)


def _first_hop_copies(shards, x_refs, zones, send_sems, recv_sems):
    ns = len(shards)
    x, y, c = _position()
    targets = [(x, y, 1 - c)] + [(cx, cy, c) for cx, cy in _other_chips(x, y)]
    return [pltpu.make_async_remote_copy(
        src_ref=x_refs[s], dst_ref=_block_of(zones[s], 4 * x + 2 * y + c, shards[s].shape[0]),
        send_sem=send_sems.at[k * ns + s], recv_sem=recv_sems.at[k * ns + s],
        device_id=to, device_id_type=MESH_ID)
        for k, to in enumerate(targets) for s in range(ns)]


def _after_all(arrays, name):
    def body(*refs):
        refs[-1][...] = jnp.zeros_like(refs[-1])

    return pl.pallas_call(
        body, name=name,
        out_shape=jax.ShapeDtypeStruct((8, LANES), F32),
        in_specs=[pl.BlockSpec(memory_space=pl.ANY)] * len(arrays),
        out_specs=pl.BlockSpec(memory_space=pltpu.VMEM),
    )(*arrays)


def _own_blocks_placed(shards, after):
    ns = len(shards)
    x, y, c = _position()
    me = jnp.reshape(4 * x + 2 * y + c, (1,)).astype(jnp.int32)
    shards = [*shards[:-1], shards[-1] + after.astype(shards[-1].dtype)]

    def body(me_ref, *refs):
        for s in range(ns):
            refs[ns + s][...] = refs[s][...]

    return pl.pallas_call(
        body, name="own_blocks_placed",
        out_shape=tuple(jax.ShapeDtypeStruct((N_DEV * s.shape[0], s.shape[1]), s.dtype) for s in shards),
        grid_spec=pltpu.PrefetchScalarGridSpec(
            num_scalar_prefetch=1, grid=(1,),
            in_specs=[pl.BlockSpec(s.shape, lambda i, me: (0, 0)) for s in shards],
            out_specs=tuple(pl.BlockSpec(s.shape, lambda i, me: (me[0], 0)) for s in shards)),
        compiler_params=_params(dimension_semantics=("arbitrary",)),
    )(me, *shards)


def _gather_start(shards, after, name):
    ns = len(shards)
    zones = _own_blocks_placed(shards, after)

    def body(*refs):
        for cp in _first_hop_copies(shards, refs[:ns], refs[ns:2 * ns], refs[2 * ns], refs[2 * ns + 1]):
            cp.start()
        refs[-1][...] = jnp.zeros_like(refs[-1])

    out = pl.pallas_call(
        body, name=name,
        out_shape=(pltpu.SemaphoreType.DMA((4 * ns,)), pltpu.SemaphoreType.DMA((4 * ns,)),
                   *[pltpu.HBM(a.shape, a.dtype) for a in (*shards, *zones)],
                   jax.ShapeDtypeStruct((8, LANES), F32)),
        in_specs=[HBM] * (2 * ns),
        out_specs=(SEM, SEM, *[HBM] * (2 * ns), pl.BlockSpec(memory_space=pltpu.VMEM)),
        input_output_aliases={i: 2 + i for i in range(2 * ns)},
        compiler_params=pltpu.CompilerParams(has_side_effects=EFFECT),
    )(*[_in_hbm(a) for a in shards], *[_in_hbm(a) for a in zones])
    return out[0], out[1], out[2:2 + ns], out[2 + ns:2 + 2 * ns], out[-1]


def _gather_wait(send_sems, recv_sems, shards, zones, after, name):
    ns = len(shards)

    def body(*refs):
        for cp in _first_hop_copies(shards, refs[:ns], refs[ns:2 * ns], refs[2 * ns], refs[2 * ns + 1]):
            cp.wait_send()
            cp.wait_recv()

    out = pl.pallas_call(
        body, name=name,
        out_shape=tuple(pltpu.HBM(a.shape, a.dtype) for a in (*shards, *zones)),
        in_specs=[HBM] * (2 * ns) + [SEM, SEM, ANY],
        out_specs=(HBM,) * (2 * ns),
        input_output_aliases={i: i for i in range(2 * ns)},
        compiler_params=pltpu.CompilerParams(has_side_effects=EFFECT),
    )(*shards, *zones, send_sems, recv_sems, after)
    return out[ns:]


def _gather_pass_on(rows, zones, name):
    ns = len(zones)

    def body(*refs):
        in_refs, out_refs = refs[:ns], refs[ns:2 * ns]
        send_sems, recv_sems = refs[2 * ns:]
        x, y, c = _position()
        copies = [pltpu.make_async_remote_copy(
            src_ref=_block_of(in_refs[s], 4 * cx + 2 * cy + c, rows[s]),
            dst_ref=_block_of(out_refs[s], 4 * cx + 2 * cy + c, rows[s]),
            send_sem=send_sems.at[j * ns + s], recv_sem=recv_sems.at[j * ns + s],
            device_id=(x, y, 1 - c), device_id_type=MESH_ID)
            for j, (cx, cy) in enumerate(_other_chips(x, y)) for s in range(ns)]
        for cp in copies:
            cp.start()
        for cp in copies:
            cp.wait_recv()
        for cp in copies:
            cp.wait_send()

    return pl.pallas_call(
        body, name=name,
        out_shape=tuple(jax.ShapeDtypeStruct(z.shape, z.dtype) for z in zones),
        in_specs=[ANY] * ns, out_specs=(ANY,) * ns,
        input_output_aliases={i: i for i in range(ns)},
        scratch_shapes=[pltpu.SemaphoreType.DMA((3 * ns,)), pltpu.SemaphoreType.DMA((3 * ns,))],
    )(*zones)


def _pair_exchange(sections, grads, name):
    ns = len(sections)

    def body(*refs):
        g_refs, land = refs[:ns], refs[ns:2 * ns]
        send_sems, recv_sems = refs[2 * ns:]
        x, y, c = _position()
        copies = [pltpu.make_async_remote_copy(
            src_ref=_block_of(g_refs[s], 2 * k + 1 - c, rows), dst_ref=land[s].at[k],
            send_sem=send_sems.at[k * ns + s], recv_sem=recv_sems.at[k * ns + s],
            device_id=(x, y, 1 - c), device_id_type=MESH_ID)
            for k in range(N_CHIPS) for s, (_, rows, _) in enumerate(sections)]
        for cp in copies:
            cp.start()
        for cp in copies:
            cp.wait_recv()
        for cp in copies:
            cp.wait_send()

    n = N_CHIPS * ns
    return pl.pallas_call(
        body, name=name,
        out_shape=tuple(jax.ShapeDtypeStruct((N_CHIPS, rows, cols), BF16) for _, rows, cols in sections),
        in_specs=[ANY] * ns, out_specs=(ANY,) * ns,
        scratch_shapes=[pltpu.SemaphoreType.DMA((n,)), pltpu.SemaphoreType.DMA((n,))],
    )(*grads)


def _pair_add(sections, grads, got, core, name):
    ns = len(sections)

    def body(core_ref, *refs):
        g_refs, got_refs, p_refs = refs[:ns], refs[ns:2 * ns], refs[2 * ns:]
        for s in range(ns):
            p_refs[s][0] = (g_refs[s][...].astype(F32) + got_refs[s][0].astype(F32)).astype(BF16)

    slot = [pl.BlockSpec((1, rows, cols), lambda k, c: (k, 0, 0)) for _, rows, cols in sections]
    return pl.pallas_call(
        body, name=name,
        out_shape=tuple(jax.ShapeDtypeStruct((N_CHIPS, rows, cols), BF16) for _, rows, cols in sections),
        grid_spec=pltpu.PrefetchScalarGridSpec(
            num_scalar_prefetch=1, grid=(N_CHIPS,),
            in_specs=[pl.BlockSpec((rows, cols), lambda k, c: (2 * k + c[0], 0)) for _, rows, cols in sections]
            + slot,
            out_specs=tuple(slot)),
        compiler_params=_params(dimension_semantics=("parallel",)),
    )(core, *grads, *got)


def _chip_copies(sections, p_refs, land, send_sems, recv_sems):
    ns = len(sections)
    x, y, c = _position()
    return [pltpu.make_async_remote_copy(
        src_ref=p_refs[s].at[2 * cx + cy], dst_ref=land[s].at[j],
        send_sem=send_sems.at[j * ns + s], recv_sem=recv_sems.at[j * ns + s],
        device_id=(cx, cy, c), device_id_type=MESH_ID)
        for j, (cx, cy) in enumerate(_other_chips(x, y)) for s in range(ns)]


def _chip_exchange(sections, parts, name):
    ns = len(sections)

    def body(*refs):
        copies = _chip_copies(sections, refs[:ns], refs[ns:2 * ns], *refs[2 * ns:])
        for cp in copies:
            cp.start()
        for cp in copies:
            cp.wait_recv()
        for cp in copies:
            cp.wait_send()

    n = 3 * ns
    return pl.pallas_call(
        body, name=name,
        out_shape=tuple(jax.ShapeDtypeStruct((3, rows, cols), BF16) for _, rows, cols in sections),
        in_specs=[ANY] * ns, out_specs=(ANY,) * ns,
        scratch_shapes=[pltpu.SemaphoreType.DMA((n,)), pltpu.SemaphoreType.DMA((n,))],
    )(*parts)


def _chip_exchange_start(sections, parts, name):
    ns = len(sections)

    def body(*refs):
        p_refs, land = refs[:ns], refs[ns:2 * ns]
        send_sems, recv_sems = refs[2 * ns], refs[2 * ns + 1]
        token = refs[-1]
        for cp in _chip_copies(sections, p_refs, land, send_sems, recv_sems):
            cp.start()
        token[...] = jnp.zeros_like(token)

    zones = [lax.empty((3, rows, cols), BF16) for _, rows, cols in sections]
    out = pl.pallas_call(
        body, name=name,
        out_shape=(pltpu.SemaphoreType.DMA((3 * ns,)), pltpu.SemaphoreType.DMA((3 * ns,)),
                   *[pltpu.HBM(a.shape, a.dtype) for a in parts], *[pltpu.HBM(a.shape, a.dtype) for a in zones],
                   jax.ShapeDtypeStruct((8, LANES), F32)),
        in_specs=[HBM] * (2 * ns),
        out_specs=(SEM, SEM, *[HBM] * (2 * ns), pl.BlockSpec(memory_space=pltpu.VMEM)),
        input_output_aliases={i: 2 + i for i in range(2 * ns)},
        compiler_params=pltpu.CompilerParams(has_side_effects=EFFECT),
    )(*[_in_hbm(a) for a in parts], *[_in_hbm(a) for a in zones])
    return out[0], out[1], out[2:2 + ns], out[2 + ns:2 + 2 * ns], out[-1]


def _chip_exchange_wait(sections, send_sems, recv_sems, parts, zones, after, name):
    ns = len(sections)

    def body(*refs):
        p_refs, land = refs[:ns], refs[ns:2 * ns]
        for cp in _chip_copies(sections, p_refs, land, refs[2 * ns], refs[2 * ns + 1]):
            cp.wait_send()
            cp.wait_recv()

    out = pl.pallas_call(
        body, name=name,
        out_shape=tuple(pltpu.HBM(a.shape, a.dtype) for a in (*parts, *zones)),
        in_specs=[HBM] * (2 * ns) + [SEM, SEM, ANY],
        out_specs=(HBM,) * (2 * ns),
        input_output_aliases={i: i for i in range(2 * ns)},
        compiler_params=pltpu.CompilerParams(has_side_effects=EFFECT),
    )(*parts, *zones, send_sems, recv_sems, after)
    return out[:ns], out[ns:]


def _grad_finish(sections, parts, far, chip, name):
    ns = len(sections)

    def body(chip_ref, *refs):
        p_refs, b_refs, g_refs = refs[:ns], refs[ns:2 * ns], refs[2 * ns:]
        for s in range(ns):
            g = p_refs[s][0].astype(F32)
            for j in range(3):
                g = g + b_refs[s][j].astype(F32)
            g_refs[s][...] = g

    half = [(rows // 2, cols) for _, rows, cols in sections]
    return pl.pallas_call(
        body, name=name,
        out_shape=tuple(jax.ShapeDtypeStruct((rows, cols), F32) for _, rows, cols in sections),
        grid_spec=pltpu.PrefetchScalarGridSpec(
            num_scalar_prefetch=1, grid=(2,),
            in_specs=[pl.BlockSpec((1, r, c), lambda i, chip: (chip[0], i, 0)) for r, c in half]
            + [pl.BlockSpec((3, r, c), lambda i, chip: (0, i, 0)) for r, c in half],
            out_specs=tuple(pl.BlockSpec((r, c), lambda i, chip: (i, 0)) for r, c in half)),
        compiler_params=_params(dimension_semantics=("parallel",)),
    )(chip, *parts, *far)


def _sum_devices(parts, rows, name):
    cols = parts.shape[1]
    tr = rows // 2

    def body(*refs):
        s = refs[0][...].astype(F32)
        for d in range(1, N_DEV):
            s = s + refs[d][...].astype(F32)
        refs[N_DEV][...] = s

    return pl.pallas_call(
        body, name=name,
        out_shape=jax.ShapeDtypeStruct((rows, cols), F32),
        grid=(2,),
        in_specs=[pl.BlockSpec((tr, cols), lambda i, d=d: (2 * d + i, 0)) for d in range(N_DEV)],
        out_specs=pl.BlockSpec((tr, cols), lambda i: (i, 0)),
        compiler_params=_params(dimension_semantics=("parallel",)),
    )(*([parts] * N_DEV))


def _adamw_step(w_ref, g_ref, m_ref, v_ref, d_ref, nm_ref, nv_ref):
    c1 = 1.0 / (1.0 - ADAM_B1 ** ADAM_STEP)
    c2 = 1.0 / (1.0 - ADAM_B2 ** ADAM_STEP)
    gv = g_ref[...]
    nm = ADAM_B1 * m_ref[...] + (1.0 - ADAM_B1) * gv
    nv = ADAM_B2 * v_ref[...] + (1.0 - ADAM_B2) * (gv * gv)
    nm_ref[...] = nm
    nv_ref[...] = nv
    d_ref[...] = (-ADAM_LR) * ((nm * c1) / (jnp.sqrt(nv * c2) + ADAM_EPS) + ADAM_WD * w_ref[...])


def _adamw_small(params, name):
    n = len(params)

    def body(*refs):
        for k in range(n):
            _adamw_step(*refs[4 * k:4 * k + 4], *refs[4 * n + 3 * k:4 * n + 3 * k + 3])

    out = pl.pallas_call(
        body, name=name,
        out_shape=tuple(jax.ShapeDtypeStruct(p[0].shape, F32) for p in params for _ in range(3)),
    )(*[a for p in params for a in p])
    return [out[3 * k:3 * k + 3] for k in range(n)]


def _adamw(w, g, m, v, name):
    rows, cols = w.shape
    tr = rows
    while tr * cols * 4 > (1 << 20) and tr % 16 == 0:
        tr //= 2

    def body(*refs):
        _adamw_step(*refs)

    spec = pl.BlockSpec((tr, cols), lambda i: (i, 0))
    shape = jax.ShapeDtypeStruct((rows, cols), F32)
    return pl.pallas_call(
        body, name=name,
        out_shape=(shape, shape, shape),
        grid=(rows // tr,),
        in_specs=[spec] * 4, out_specs=(spec,) * 3,
        compiler_params=_params(dimension_semantics=("parallel",)),
    )(w, g, m, v)


NAMES = ("ln1_g", "w_in", "b_in", "rpb", "w_att_o", "conv_w", "conv_b", "w_rg_a", "b_rg_a", "w_rg_i",
         "b_rg_i", "lru_lambda", "w_rec_o", "w_out", "ln2_g", "w_ff1", "w_ff2", "lnf_g")
TRANSPOSED = {"w_in": "w_in_t", "w_att_o": "w_att_o_t", "w_ff1": "w_ff1_t"}
ROW_SHARDED = ("w_rec_o", "w_out", "w_ff2")
REPLICATED = (("ln1_g", (1, D)), ("b_in", (1, D_IN)), ("rpb", (N_HEADS * N_RPB_R, N_RPB_C)),
              ("conv_b", (1, D_REC)), ("w_rg_a", (2 * N_REC_BLOCKS * REC_BLOCK, REC_BLOCK)),
              ("w_rg_i", (2 * N_REC_BLOCKS * REC_BLOCK, REC_BLOCK)), ("ln2_g", (1, D)), ("lnf_g", (1, D)))
GATE_BLOCKS = ("w_rg_a", "w_rg_i")
SMALL_ROWS = 112


def _chan_bits(vectors):
    chan = jnp.concatenate(vectors, axis=0)
    bits = lax.bitcast_convert_type(chan, BF16).reshape(-1)
    return jnp.pad(bits, (0, CHAN_BLOCK_ROWS * D - bits.shape[0])).reshape(CHAN_BLOCK_ROWS, D)


def _chan_from_bits(gathered):
    bits = gathered.reshape(N_DEV, CHAN_BLOCK_ROWS * D)[:, :2 * N_CHAN_ROWS * LANES]
    chan = lax.bitcast_convert_type(bits.reshape(N_DEV, N_CHAN_ROWS, LANES, 2), F32)
    return chan.transpose(1, 0, 2).reshape(N_CHAN_ROWS, D)


def kernel(x, ln1_g, w_in, b_in, rpb, w_att_o, conv_w, conv_b, w_rg_a, b_rg_a, w_rg_i, b_rg_i, lru_lambda, w_rec_o, w_out, ln2_g, w_ff1, w_ff2, lnf_g, loss_target, m_ln1_g, m_w_in, m_b_in, m_rpb, m_w_att_o, m_conv_w, m_conv_b, m_w_rg_a, m_b_rg_a, m_w_rg_i, m_b_rg_i, m_lru_lambda, m_w_rec_o, m_w_out, m_ln2_g, m_w_ff1, m_w_ff2, m_lnf_g, v_ln1_g, v_w_in, v_b_in, v_rpb, v_w_att_o, v_conv_w, v_conv_b, v_w_rg_a, v_b_rg_a, v_w_rg_i, v_b_rg_i, v_lru_lambda, v_w_rec_o, v_w_out, v_ln2_g, v_w_ff1, v_w_ff2, v_lnf_g):
    w = dict(zip(NAMES, (ln1_g, w_in, b_in, rpb, w_att_o, conv_w, conv_b, w_rg_a, b_rg_a, w_rg_i,
                         b_rg_i, lru_lambda, w_rec_o, w_out, ln2_g, w_ff1, w_ff2, lnf_g)))
    m = dict(zip(NAMES, (m_ln1_g, m_w_in, m_b_in, m_rpb, m_w_att_o, m_conv_w, m_conv_b, m_w_rg_a,
                         m_b_rg_a, m_w_rg_i, m_b_rg_i, m_lru_lambda, m_w_rec_o, m_w_out, m_ln2_g,
                         m_w_ff1, m_w_ff2, m_lnf_g)))
    v = dict(zip(NAMES, (v_ln1_g, v_w_in, v_b_in, v_rpb, v_w_att_o, v_conv_w, v_conv_b, v_w_rg_a,
                         v_b_rg_a, v_w_rg_i, v_b_rg_i, v_lru_lambda, v_w_rec_o, v_w_out, v_ln2_g,
                         v_w_ff1, v_w_ff2, v_lnf_g)))
    xi, yi, ci = _position()

    shard = {t: w[n][0].T.astype(BF16) for n, t in TRANSPOSED.items()}
    shard.update({n: w[n][0].astype(BF16) for n in ROW_SHARDED})
    shard["chan"] = _chan_bits([w[n][0] for n, _ in CHAN])
    first, later = ("w_in_t", "chan"), ("w_rec_o", "w_out", "w_ff1_t", "w_ff2", "w_att_o_t")
    *gathered, done = _all_gather([shard[n] for n in first], "weight_all_gather")
    p = dict(zip(first, gathered))
    send_sems, recv_sems, sent, zones, token = _gather_start([shard[n] for n in later], done,
                                                             "weight_gather_start")

    def late_weights(after):
        landed = _gather_wait(send_sems, recv_sems, sent, zones, after, "weight_gather_wait")
        return dict(zip(later, _gather_pass_on([shard[n].shape[0] for n in later], landed,
                                               "weight_gather_pass_on")))

    chan = _chan_from_bits(p.pop("chan"))
    r0 = 0
    for n, rows in CHAN:
        p[n] = chan[r0:r0 + rows]
        r0 += rows
    p.update(ln1_g=w["ln1_g"], b_in=w["b_in"] + token[0, 0], rpb=w["rpb"][0], conv_b=w["conv_b"],
             w_rg_a=w["w_rg_a"][0], w_rg_i=w["w_rg_i"][0], ln2_g=w["ln2_g"],
             lnf_g=w["lnf_g"].reshape(1, D))

    core = jnp.reshape(ci, (1,)).astype(jnp.int32)
    chip = jnp.reshape(2 * xi + yi, (1,)).astype(jnp.int32)
    early_sections, late_sections = SECTIONS[1:], SECTIONS[:1]
    in_flight = {}

    def reduce_early(grads):
        chan_g = jnp.concatenate([grads[n] for n, _ in CHAN], axis=0)
        chan_g = chan_g.reshape(N_CHAN_ROWS, N_DEV, LANES).transpose(1, 0, 2).astype(BF16)
        chan_g = jnp.pad(chan_g.reshape(N_DEV, -1), ((0, 0), (0, CHAN_BLOCK_ROWS * D - N_CHAN_ROWS * LANES)))
        grads["chan"] = chan_g.reshape(N_DEV * CHAN_BLOCK_ROWS, D)
        grads["gates"] = jnp.concatenate([grads[n].reshape(-1, D) for n in GATE_BLOCKS], axis=0).astype(BF16)
        sect = [grads[n] for n, _, _ in early_sections]
        got = _pair_exchange(early_sections, sect, "grad_pair_exchange_early")
        parts = _pair_add(early_sections, sect, got, core, "grad_pair_add_early")
        in_flight["early"] = _chip_exchange_start(early_sections, parts, "grad_chip_exchange_start")
        return in_flight["early"][-1][0, 0]

    loss_part, grad_x, grads = _local_step(x[0], loss_target[0], p, late_weights, reduce_early)
    sect = [grads[n] for n, _, _ in late_sections]
    got = _pair_exchange(late_sections, sect, "grad_pair_exchange_late")
    late_parts = _pair_add(late_sections, sect, got, core, "grad_pair_add_late")
    in_flight["late"] = _chip_exchange_start(late_sections, late_parts, "grad_chip_exchange_start_late")

    def finish(group, sections, after, name):
        send_sems, recv_sems, parts, zones, _ = in_flight[group]
        parts, far = _chip_exchange_wait(sections, send_sems, recv_sems, parts, zones, after,
                                         "grad_chip_exchange_wait_" + name)
        return dict(zip((n for n, _, _ in sections),
                        _grad_finish(sections, parts, far, chip, "grad_finish_" + name)))

    started_late = in_flight["late"][-1]
    summed = finish("early", early_sections, started_late, "early")

    flat = jnp.concatenate([grads[n].reshape(-1) for n, _ in REPLICATED if n not in GATE_BLOCKS]
                           + [loss_part.reshape(-1) + started_late[0, 0]])
    n_small = flat.shape[0]
    flat = jnp.pad(flat, (0, SMALL_ROWS * LANES - n_small)).reshape(SMALL_ROWS, LANES)
    small_parts, gate_sum, _ = _all_gather([flat, summed["gates"]], "small_grad_all_gather")
    small = _sum_devices(small_parts, SMALL_ROWS, "small_grad_sum").reshape(-1)
    loss = small[n_small - 1]

    g, delta, new_m, new_v = {}, {}, {}, {}

    def update(n, g2, shape2):
        d2, m2, v2 = _adamw(w[n].reshape(shape2), g2, m[n].reshape(shape2), v[n].reshape(shape2),
                            "adamw_" + n)
        g[n], delta[n], new_m[n], new_v[n] = (a.reshape(w[n].shape) for a in (g2, d2, m2, v2))

    small_params = []
    o = 0
    for n, shape2 in REPLICATED:
        if n in GATE_BLOCKS:
            k, rows = GATE_BLOCKS.index(n), gate_sum.shape[0] // len(GATE_BLOCKS)
            update(n, gate_sum[k * rows:(k + 1) * rows].reshape(shape2), shape2)
        else:
            size = shape2[0] * shape2[1]
            small_params.append((n, small[o:o + size].reshape(shape2), shape2))
            o += size
    chan_back = summed["chan"].reshape(-1)[:N_CHAN_ROWS * LANES].reshape(N_CHAN_ROWS, LANES)
    r0 = 0
    for n, rows in CHAN:
        small_params.append((n, chan_back[r0:r0 + rows], (rows, LANES)))
        r0 += rows
    results = _adamw_small([(w[n].reshape(s2), g2, m[n].reshape(s2), v[n].reshape(s2))
                            for n, g2, s2 in small_params], "adamw_vectors")
    for (n, g2, _), (d2, m2, v2) in zip(small_params, results):
        g[n], delta[n], new_m[n], new_v[n] = (a.reshape(w[n].shape) for a in (g2, d2, m2, v2))

    for n in ROW_SHARDED:
        update(n, summed[n], summed[n].shape)
    for n, t in TRANSPOSED.items():
        if t in summed:
            update(n, summed[t].T, summed[t].shape[::-1])
    summed = finish("late", late_sections, _after_all(list(delta.values()), "updates_done"), "late")
    update("w_in", summed["w_in_t"].T, summed["w_in_t"].shape[::-1])

    return (loss, grad_x[None], *[g[n] for n in NAMES], *[delta[n] for n in NAMES],
            *[new_m[n] for n in NAMES], *[new_v[n] for n in NAMES])
```

```python
import math

import numpy as np
import jax
import jax.numpy as jnp
from jax import lax
from jax.experimental import pallas as pl
from jax.experimental.pallas import tpu as pltpu

F32 = jnp.float32
BF16 = jnp.bfloat16

T = 2048
D = 1024
D_ATT = 512
D_REC = 1024
D_FF = 4096
D_IN = 5632
N_HEADS = 8
DH = 64
GRID_W = 64
ROWS = T // GRID_W
WIN_H = 8
WIN_W = 16
KWIN = WIN_H * GRID_W
N_RPB_R = 2 * WIN_H - 1
N_RPB_C = 2 * WIN_W - 1
N_REC_BLOCKS = 16
REC_BLOCK = 64
CG = 128
N_CG = D_REC // CG
LRU_C = 8.0
EPS = 1e-6
N_DEV = 8
N_CHIPS = 4
LANES = 128

ADAM_LR = 0.001
ADAM_B1 = 0.9
ADAM_B2 = 0.999
ADAM_EPS = 1e-08
ADAM_WD = 0.01
ADAM_STEP = 10

MESH_AXES = ("x", "y", "c")
VMEM_LIMIT = 56 * 1024 * 1024

TILE = 512
DZ_ARRAYS = ((0, 3, 1), (3, 4, 2), (7, 4, 2))
N_DZ_TILES = D_IN // TILE


def _params(**kw):
    return pltpu.CompilerParams(vmem_limit_bytes=VMEM_LIMIT, **kw)


def _att_tables():
    rq = np.arange(2 * GRID_W) % GRID_W
    kc = np.arange(KWIN) % GRID_W
    win_start = np.clip(rq - WIN_W // 2, 0, GRID_W - WIN_W)
    valid = (kc[None, :] >= win_start[:, None]) & (kc[None, :] < win_start[:, None] + WIN_W)
    return valid.astype(np.float32), _pair_mask()


def _pair_mask():
    half = np.arange(2 * DH) // DH
    return (half[:, None] == half[None, :]).astype(np.float32)


def _dup_table():
    return np.concatenate([np.eye(REC_BLOCK, dtype=np.float32)] * 2, axis=1)


def _sigmoid(x):
    return 0.5 * jnp.tanh(0.5 * x) + 0.5


def _softplus(x):
    return jnp.maximum(x, 0.0) + jnp.log(1.0 + jnp.exp(-jnp.abs(x)))


def _one_minus_square(log_a, a):
    x = 2.0 * log_a
    series = -x * (1.0 + x * (0.5 + x * (1.0 / 6.0)))
    return jnp.where(x > -0.02, series, 1.0 - a * a)


_GELU_C = math.sqrt(2.0 / math.pi)


def _gelu_and_grad(x):
    x2 = x * x
    inner = _GELU_C * (x + 0.044715 * x * x2)
    t = jnp.tanh(inner)
    g = 0.5 * x * (1.0 + t)
    dg = 0.5 * (1.0 + t) + 0.5 * x * (1.0 - t * t) * _GELU_C * (1.0 + 3.0 * 0.044715 * x2)
    return g, dg


def _dot(a, b):
    return jnp.dot(a, b, preferred_element_type=F32)


def _dot_nt(a, b):
    return lax.dot_general(a, b, (((1,), (1,)), ((), ())), preferred_element_type=F32)


def _dot_tn(a, b):
    return lax.dot_general(a, b, (((0,), (0,)), ((), ())), preferred_element_type=F32)


def _dot_exact(a, b):
    return jnp.dot(a, b, precision=lax.Precision.HIGHEST, preferred_element_type=F32)


def _shift_rows(x, s):
    n = x.shape[0]
    rows = lax.broadcasted_iota(jnp.int32, x.shape, 0)
    y = pltpu.roll(x, s % n, 0)
    if s > 0:
        return jnp.where(rows >= s, y, 0.0)
    return jnp.where(rows < n + s, y, 0.0)


def _rms_bwd(dh, xh, r, g):
    dxh = dh * g
    return r * (dxh - xh * jnp.mean(dxh * xh, axis=-1, keepdims=True))


def _matmul(a, b, mode, out_dtype, name, tm=512, tn=1024, tk=2048):
    if mode == "nn":
        (m, k), (k2, n) = a.shape, b.shape
    elif mode == "nt":
        (m, k), (n, k2) = a.shape, b.shape
    else:
        (k, m), (k2, n) = a.shape, b.shape
    assert k == k2
    tm, tn, tk = min(tm, m), min(tn, n), min(tk, k)
    assert m % tm == 0 and n % tn == 0 and k % tk == 0
    nk = k // tk
    dot = {"nn": _dot, "nt": _dot_nt, "tn": _dot_tn}[mode]

    def body(a_ref, b_ref, o_ref, acc):
        kk = pl.program_id(2)
        part = dot(a_ref[...].astype(BF16), b_ref[...].astype(BF16))
        if nk == 1:
            o_ref[...] = part.astype(out_dtype)
            return

        @pl.when(kk == 0)
        def _():
            acc[...] = part

        @pl.when(kk > 0)
        def _():
            acc[...] += part

        @pl.when(kk == nk - 1)
        def _():
            o_ref[...] = acc[...].astype(out_dtype)

    if mode == "tn":
        a_spec = pl.BlockSpec((tk, tm), lambda i, j, kk: (kk, i))
    else:
        a_spec = pl.BlockSpec((tm, tk), lambda i, j, kk: (i, kk))
    if mode == "nt":
        b_spec = pl.BlockSpec((tn, tk), lambda i, j, kk: (j, kk))
    else:
        b_spec = pl.BlockSpec((tk, tn), lambda i, j, kk: (kk, j))
    return pl.pallas_call(
        body, name=name,
        out_shape=jax.ShapeDtypeStruct((m, n), out_dtype),
        grid=(m // tm, n // tn, nk),
        in_specs=[a_spec, b_spec],
        out_specs=pl.BlockSpec((tm, tn), lambda i, j, kk: (i, j)),
        scratch_shapes=[pltpu.VMEM((tm, tn) if nk > 1 else (8, LANES), F32)],
        compiler_params=_params(dimension_semantics=("parallel", "parallel", "arbitrary")),
    )(a, b)


def _in_proj(x, g1, w_in_t, b_in):
    tm = 1024

    def body(x_ref, g_ref, w_ref, b_ref, qkv_ref, uy_ref, gg_ref, h_ref, h_scr):
        j = pl.program_id(1)

        @pl.when(j == 0)
        def _():
            xv = x_ref[...]
            r = lax.rsqrt(jnp.mean(xv * xv, axis=-1, keepdims=True) + EPS)
            h = ((xv * r) * g_ref[...]).astype(BF16)
            h_scr[...] = h
            h_ref[...] = h

        z = _dot_nt(h_scr[...], w_ref[...]) + b_ref[...]

        @pl.when(j < 3)
        def _():
            qkv_ref[...] = z.astype(BF16)

        @pl.when((j >= 3) & (j < 7))
        def _():
            uy_ref[...] = z

        @pl.when(j >= 7)
        def _():
            gg_ref[...] = z

    return pl.pallas_call(
        body, name="in_proj",
        out_shape=(jax.ShapeDtypeStruct((T, 3 * D_ATT), BF16),
                   jax.ShapeDtypeStruct((T, 2 * D_REC), F32),
                   jax.ShapeDtypeStruct((T, 2 * D), F32),
                   jax.ShapeDtypeStruct((T, D), BF16)),
        grid=(T // tm, N_DZ_TILES),
        in_specs=[pl.BlockSpec((tm, D), lambda i, j: (i, 0)),
                  pl.BlockSpec((1, D), lambda i, j: (0, 0)),
                  pl.BlockSpec((TILE, D), lambda i, j: (j, 0)),
                  pl.BlockSpec((1, TILE), lambda i, j: (0, j))],
        out_specs=(pl.BlockSpec((tm, TILE), lambda i, j: (i, jnp.minimum(j, 2))),
                   pl.BlockSpec((tm, TILE), lambda i, j: (i, jnp.clip(j - 3, 0, 3))),
                   pl.BlockSpec((tm, TILE), lambda i, j: (i, jnp.clip(j - 7, 0, 3))),
                   pl.BlockSpec((tm, D), lambda i, j: (i, 0))),
        scratch_shapes=[pltpu.VMEM((tm, D), BF16)],
        compiler_params=_params(dimension_semantics=("parallel", "arbitrary")),
    )(x, g1, w_in_t, b_in)


def _dz_specs(rows, tile_of, row_of):
    def spec(off, n, per_plane):
        def index(*ids):
            t = jnp.clip(tile_of(*ids) - off, 0, n - 1)
            return (t // per_plane, row_of(*ids), t % per_plane)
        return pl.BlockSpec((1, rows, TILE), index)
    return [spec(off, n, per) for off, n, per in DZ_ARRAYS]


def _dh_norm1_bwd(dz, w_in_t, x, g1, dx1):
    tm = 1024

    def body(*refs):
        seg_refs = refs[:3]
        w_ref, x_ref, g_ref, dx1_ref, gx_ref, dg_ref, acc = refs[3:]
        i, kk = pl.program_id(0), pl.program_id(1)

        @pl.when(kk == 0)
        def _():
            acc[...] = jnp.zeros_like(acc)

        for s, (off, n, _) in enumerate(DZ_ARRAYS):
            @pl.when((kk >= off) & (kk < off + n))
            def _(s=s):
                acc[...] += _dot(seg_refs[s][0], w_ref[...])

        @pl.when((i == 0) & (kk == 0))
        def _():
            dg_ref[...] = jnp.zeros_like(dg_ref)

        @pl.when(kk == N_DZ_TILES - 1)
        def _():
            xv = x_ref[...]
            r = lax.rsqrt(jnp.mean(xv * xv, axis=-1, keepdims=True) + EPS)
            xh = xv * r
            dh = acc[...]
            dg_ref[...] += jnp.sum(dh * xh, axis=0, keepdims=True)
            gx_ref[...] = dx1_ref[...] + _rms_bwd(dh, xh, r, g_ref[...])

    tok = pl.BlockSpec((tm, D), lambda i, j: (i, 0))
    vec = pl.BlockSpec((1, D), lambda i, j: (0, 0))
    return pl.pallas_call(
        body, name="dh_norm1_bwd",
        out_shape=(jax.ShapeDtypeStruct((T, D), F32), jax.ShapeDtypeStruct((1, D), F32)),
        grid=(T // tm, N_DZ_TILES),
        in_specs=_dz_specs(tm, lambda i, j: j, lambda i, j: i)
        + [pl.BlockSpec((TILE, D), lambda i, j: (j, 0)), tok, vec, tok],
        out_specs=(tok, vec),
        scratch_shapes=[pltpu.VMEM((tm, D), F32)],
        compiler_params=_params(dimension_semantics=("arbitrary", "arbitrary")),
    )(*dz, w_in_t, x, g1, dx1)


def _grad_w_in(dz, h):
    def body(*refs):
        seg_refs = refs[:3]
        h_ref, gw_ref, gb_ref = refs[3:]
        j = pl.program_id(0)

        for s, (off, n, _) in enumerate(DZ_ARRAYS):
            @pl.when((j >= off) & (j < off + n))
            def _(s=s):
                a = seg_refs[s][0]
                gw_ref[...] = _dot_tn(a, h_ref[...]).astype(BF16)
                gb_ref[...] = jnp.sum(a.astype(F32), axis=0, keepdims=True)

    return pl.pallas_call(
        body, name="grad_w_in",
        out_shape=(jax.ShapeDtypeStruct((D_IN, D), BF16), jax.ShapeDtypeStruct((1, D_IN), F32)),
        grid=(N_DZ_TILES,),
        in_specs=_dz_specs(T, lambda j: j, lambda j: 0) + [pl.BlockSpec((T, D), lambda j: (0, 0))],
        out_specs=(pl.BlockSpec((TILE, D), lambda j: (j, 0)), pl.BlockSpec((1, TILE), lambda j: (0, j))),
        compiler_params=_params(dimension_semantics=("parallel",)),
    )(*dz, h)


def _rpb_rows(rpb):
    padded = jnp.pad(rpb, ((0, 0), (0, 0), (0, GRID_W - N_RPB_C)))
    rows = [padded[:, WIN_H - 1 - oi: 2 * WIN_H - 1 - oi].reshape(N_HEADS // 2, 2, KWIN)
            for oi in range(WIN_H)]
    return jnp.stack(rows, axis=0)


SKEW = KWIN - (WIN_W - 1)


MASKED = -1e30


def _bias_tiles(rows_ref, valid, bias_s):
    for oi in range(WIN_H):
        for hh in range(2):
            row = jnp.broadcast_to(rows_ref[oi, 0, hh:hh + 1, :], (GRID_W, KWIN))
            tile = pltpu.roll(row, SKEW, 1, stride=1, stride_axis=0)
            bias_s[oi, hh * GRID_W:(hh + 1) * GRID_W, :] = jnp.where(valid[:GRID_W], tile, MASKED)


def _bias_tile_grads(gb_s, flip, out_ref):
    for oi in range(WIN_H):
        for hh in range(2):
            g = _dot_exact(flip, gb_s[oi, hh * GRID_W:(hh + 1) * GRID_W, :])
            back = pltpu.roll(g, KWIN - (GRID_W - WIN_W), 1, stride=1, stride_axis=0)
            out_ref[0, oi, hh:hh + 1, :] = jnp.sum(back, axis=0, keepdims=True)


def _rpb_fold(row_grads):
    g = row_grads.transpose(1, 0, 2, 3).reshape(WIN_H, N_HEADS, WIN_H, GRID_W)
    g = g.transpose(0, 2, 1, 3)

    def body(g_ref, o_ref):
        for dr in range(N_RPB_R):
            terms = [g_ref[oi, i] for oi in range(WIN_H) for i in range(WIN_H) if i - oi + WIN_H - 1 == dr]
            acc = terms[0]
            for term in terms[1:]:
                acc = acc + term
            o_ref[dr] = acc

    out = pl.pallas_call(
        body, name="rpb_fold",
        out_shape=jax.ShapeDtypeStruct((N_RPB_R, N_HEADS, GRID_W), F32),
    )(g)
    return out.transpose(1, 0, 2)[:, :, :N_RPB_C]


def _att_scores(q_ref, k_ref, bias_ref, hmask, r):
    rs = jnp.clip(r - WIN_H // 2, 0, ROWS - WIN_H)
    oi = r - rs
    q0 = pl.multiple_of(r * GRID_W, GRID_W)
    k0 = pl.multiple_of(rs * GRID_W, GRID_W)
    q_r = q_ref[pl.ds(q0, GRID_W), :] * (DH ** -0.5)
    q2 = jnp.where(hmask, jnp.concatenate([q_r, q_r], axis=0), jnp.zeros((), BF16))
    kw = k_ref[pl.ds(k0, KWIN), :]
    s = _dot_nt(q2, kw) + bias_ref[oi]
    e = jnp.exp(s - jnp.max(s, axis=-1, keepdims=True))
    return e, 1.0 / jnp.sum(e, axis=-1, keepdims=True), q2, kw, q0, k0, oi


def _att_fwd(qkv, bias_rows):
    valid_np, hmask_np = _att_tables()

    def body(q_ref, k_ref, v_ref, rows_ref, valid_ref, hmask_ref, o_ref, bias_s):
        valid = valid_ref[...] > 0.5
        hmask = hmask_ref[...] > 0.5
        first_head = lax.broadcasted_iota(jnp.int32, (GRID_W, 2 * DH), 1) < DH
        _bias_tiles(rows_ref, valid, bias_s)

        def row(r, carry):
            e, rl, _, _, q0, k0, _ = _att_scores(q_ref, k_ref, bias_s, hmask, r)
            o2 = _dot((e * rl).astype(BF16), v_ref[pl.ds(k0, KWIN), :])
            o_ref[pl.ds(q0, GRID_W), :] = jnp.where(first_head, o2[:GRID_W], o2[GRID_W:]).astype(BF16)
            return carry

        lax.fori_loop(0, ROWS, row, 0, unroll=4)

    col = lambda off: pl.BlockSpec((T, 2 * DH), lambda hp: (0, hp + off))
    return pl.pallas_call(
        body, name="att_fwd",
        out_shape=jax.ShapeDtypeStruct((T, D_ATT), BF16),
        grid=(N_HEADS // 2,),
        in_specs=[col(0), col(4), col(8),
                  pl.BlockSpec((WIN_H, 1, 2, KWIN), lambda hp: (0, hp, 0, 0)),
                  pl.BlockSpec((2 * GRID_W, KWIN), lambda hp: (0, 0)),
                  pl.BlockSpec((2 * DH, 2 * DH), lambda hp: (0, 0))],
        out_specs=pl.BlockSpec((T, 2 * DH), lambda hp: (0, hp)),
        scratch_shapes=[pltpu.VMEM((WIN_H, 2 * GRID_W, KWIN), F32)],
        compiler_params=_params(dimension_semantics=("parallel",)),
    )(qkv, qkv, qkv, bias_rows, jnp.asarray(valid_np), jnp.asarray(hmask_np))


def _att_bwd(qkv, bias_rows, datt, after):
    valid_np, hmask_np = _att_tables()

    def body(q_ref, k_ref, v_ref, do_ref, rows_ref, valid_ref, hmask_ref, flip_ref,
             dqkv_ref, grows_ref, dk_acc, dv_acc, bias_s, gb_s):
        valid = valid_ref[...] > 0.5
        hmask = hmask_ref[...] > 0.5
        first_head = lax.broadcasted_iota(jnp.int32, (GRID_W, 2 * DH), 1) < DH
        dk_acc[...] = jnp.zeros_like(dk_acc)
        dv_acc[...] = jnp.zeros_like(dv_acc)
        gb_s[...] = jnp.zeros_like(gb_s)
        _bias_tiles(rows_ref, valid, bias_s)

        def row(r, carry):
            e, rl, q2, kw, q0, k0, oi = _att_scores(q_ref, k_ref, bias_s, hmask, r)
            do_r = do_ref[pl.ds(q0, GRID_W), :]
            do2 = jnp.where(hmask, jnp.concatenate([do_r, do_r], axis=0), jnp.zeros((), BF16))
            vw = v_ref[pl.ds(k0, KWIN), :]
            p = e * rl
            dp = _dot_nt(do2, vw)
            ds = p * (dp - jnp.sum(dp * p, axis=-1, keepdims=True))
            p16 = p.astype(BF16)
            ds16 = ds.astype(BF16)
            dv_acc[pl.ds(k0, KWIN), :] += _dot_tn(p16, do2)
            dk_acc[pl.ds(k0, KWIN), :] += _dot_tn(ds16, q2)
            dq2 = _dot(ds16, kw) * (DH ** -0.5)
            dqkv_ref[0, pl.ds(q0, GRID_W), :] = jnp.where(first_head, dq2[:GRID_W], dq2[GRID_W:]).astype(BF16)
            gb_s[oi] += ds
            return carry

        lax.fori_loop(0, ROWS, row, 0, unroll=4)
        dqkv_ref[1] = dk_acc[...].astype(BF16)
        dqkv_ref[2] = dv_acc[...].astype(BF16)
        _bias_tile_grads(gb_s, flip_ref[...], grows_ref)

    col = lambda off: pl.BlockSpec((T, 2 * DH), lambda hp: (0, hp + off))
    tiles = pltpu.VMEM((WIN_H, 2 * GRID_W, KWIN), F32)
    return pl.pallas_call(
        body, name="att_bwd",
        out_shape=(jax.ShapeDtypeStruct((3, T, D_ATT), BF16),
                   jax.ShapeDtypeStruct((N_HEADS // 2, WIN_H, 2, KWIN), F32)),
        grid=(N_HEADS // 2,),
        in_specs=[col(0), col(4), col(8), col(0),
                  pl.BlockSpec((WIN_H, 1, 2, KWIN), lambda hp: (0, hp, 0, 0)),
                  pl.BlockSpec((2 * GRID_W, KWIN), lambda hp: (0, 0)),
                  pl.BlockSpec((2 * DH, 2 * DH), lambda hp: (0, 0)),
                  pl.BlockSpec((GRID_W, GRID_W), lambda hp: (0, 0))],
        out_specs=(pl.BlockSpec((3, T, 2 * DH), lambda hp: (0, 0, hp)),
                   pl.BlockSpec((1, WIN_H, 2, KWIN), lambda hp: (hp, 0, 0, 0))),
        scratch_shapes=[pltpu.VMEM((T, 2 * DH), F32), pltpu.VMEM((T, 2 * DH), F32), tiles, tiles],
        compiler_params=_params(dimension_semantics=("parallel",)),
    )(qkv, qkv, qkv, datt, bias_rows, jnp.asarray(valid_np) + after, jnp.asarray(hmask_np),
      jnp.asarray(np.eye(GRID_W, dtype=np.float32)[::-1].copy()))


def _conv_taps(up):
    return (_shift_rows(up, 2), _shift_rows(up, 1), up, _shift_rows(up, -1))


def _pair_block_diag(w_pair, dup, same_half):
    return jnp.where(same_half, _dot(w_pair.astype(BF16), dup), 0.0).astype(BF16)


def _gates(u, u16, wa, ba, wi, bi, lam):
    r = _sigmoid(_dot(u16, wa) + ba)
    ig = _sigmoid(_dot(u16, wi) + bi)
    sp = _softplus(-lam)
    log_a = (-LRU_C) * r * sp
    a = jnp.exp(log_a)
    mult2 = jnp.maximum(_one_minus_square(log_a, a), 0.0)
    return r, ig, sp, a, jnp.sqrt(mult2), mult2


SCAN_BLOCKS = 2


def _scans(jobs):
    c = jobs[0][0].shape[1]
    nblk = T // 8
    rows = lax.broadcasted_iota(jnp.int32, (8, c), 0)

    def block(a, b, reverse):
        for s in (1, 2, 4):
            if reverse:
                keep = rows < 8 - s
                a_s = jnp.where(keep, pltpu.roll(a, 8 - s, 0), 1.0)
                b_s = jnp.where(keep, pltpu.roll(b, 8 - s, 0), 0.0)
            else:
                keep = rows >= s
                a_s = jnp.where(keep, pltpu.roll(a, s, 0), 1.0)
                b_s = jnp.where(keep, pltpu.roll(b, s, 0), 0.0)
            b = a * b_s + b
            a = a * a_s
        return a, b

    def step(i, carry):
        out = []
        for (a_ref, b_ref, h_ref, reverse), h_prev in zip(jobs, carry):
            for u in range(SCAN_BLOCKS):
                blk = i * SCAN_BLOCKS + u
                if reverse:
                    blk = nblk - 1 - blk
                t0 = pl.multiple_of(blk * 8, 8)
                a, b = block(a_ref[pl.ds(t0, 8), :], b_ref[pl.ds(t0, 8), :], reverse)
                h = a * h_prev + b
                h_ref[pl.ds(t0, 8), :] = h
                h_prev = jnp.broadcast_to(h[0:1] if reverse else h[7:8], (8, c))
            out.append(h_prev)
        return tuple(out)

    lax.fori_loop(0, nblk // SCAN_BLOCKS, step, tuple(jnp.zeros((8, c), F32) for _ in jobs))


def _rec_specs():
    tok = lambda off: pl.BlockSpec((T, CG), lambda g: (0, g + off))
    per_ch = lambda rows: pl.BlockSpec((rows, CG), lambda g: (0, g))
    wspec = pl.BlockSpec((2, 1, CG, REC_BLOCK), lambda g: (0, g, 0, 0))
    const = lambda shape: pl.BlockSpec(shape, lambda g: (0, 0))
    return tok, per_ch, wspec, const


def _rec_fwd(uy, conv_w, conv_b, w_a, b_a, w_i, b_i, lam):
    tok, per_ch, wspec, const = _rec_specs()

    def body(up_ref, yb_ref, cw_ref, cb_ref, wa_ref, ba_ref, wi_ref, bi_ref, lam_ref, dup_ref, half_ref,
             hf_ref, hb_ref, yrec_ref, a_f, bx_f, a_b, bx_b):
        dup = dup_ref[...]
        same_half = half_ref[...] > 0.5
        taps = _conv_taps(up_ref[...])
        u = cb_ref[...]
        for j in range(4):
            u = u + taps[j] * cw_ref[j:j + 1, :]
        u16 = u.astype(BF16)
        for d, (a_s, bx_s) in enumerate(((a_f, bx_f), (a_b, bx_b))):
            wa = _pair_block_diag(wa_ref[d, 0], dup, same_half)
            wi = _pair_block_diag(wi_ref[d, 0], dup, same_half)
            _, ig, _, a, mult, _ = _gates(u, u16, wa, ba_ref[d:d + 1, :], wi, bi_ref[d:d + 1, :],
                                       lam_ref[d:d + 1, :])
            a_s[...] = a
            bx_s[...] = mult * (ig * u)
        _scans([(a_f, bx_f, hf_ref, False), (a_b, bx_b, hb_ref, True)])
        gelu, _ = _gelu_and_grad(yb_ref[...])
        yrec_ref[...] = ((hf_ref[...] + hb_ref[...]) * gelu).astype(BF16)

    return pl.pallas_call(
        body, name="rec_fwd",
        out_shape=(jax.ShapeDtypeStruct((T, D_REC), F32), jax.ShapeDtypeStruct((T, D_REC), F32),
                   jax.ShapeDtypeStruct((T, D_REC), BF16)),
        grid=(N_CG,),
        in_specs=[tok(0), tok(N_CG), per_ch(4), per_ch(1), wspec, per_ch(2), wspec, per_ch(2), per_ch(2),
                  const((REC_BLOCK, CG)), const((CG, CG))],
        out_specs=(tok(0), tok(0), tok(0)),
        scratch_shapes=[pltpu.VMEM((T, CG), F32)] * 4,
        compiler_params=_params(dimension_semantics=("parallel",)),
    )(uy, uy, conv_w, conv_b, w_a, b_a, w_i, b_i, lam,
      jnp.asarray(_dup_table(), BF16), jnp.asarray(_pair_mask()))


def _rec_bwd(uy, hf, hb, dyrec, conv_w, conv_b, w_a, b_a, w_i, b_i, lam):
    tok, per_ch, wspec, const = _rec_specs()

    def body(up_ref, yb_ref, hf_ref, hb_ref, dy_ref, cw_ref, cb_ref, wa_ref, ba_ref, wi_ref, bi_ref,
             lam_ref, dup_ref, dupt_ref, half_ref,
             duy_ref, dcw_ref, dcb_ref, dwa_ref, dba_ref, dwi_ref, dbi_ref, dlam_ref,
             a_s0, a_s1, dh_s, g_s0, g_s1):
        dup = dup_ref[...]
        dup_t = dupt_ref[...]
        same_half = half_ref[...] > 0.5
        taps = _conv_taps(up_ref[...])
        u = cb_ref[...]
        for j in range(4):
            u = u + taps[j] * cw_ref[j:j + 1, :]
        u16 = u.astype(BF16)
        gelu, dgelu = _gelu_and_grad(yb_ref[...])
        dy = dy_ref[...]
        duy_ref[1] = (dy * (hf_ref[...] + hb_ref[...]) * dgelu).astype(BF16)
        dh_s[...] = dy * gelu
        gate_values = []
        for d, a_s in enumerate((a_s0, a_s1)):
            wa = _pair_block_diag(wa_ref[d, 0], dup, same_half)
            wi = _pair_block_diag(wi_ref[d, 0], dup, same_half)
            lam_d = lam_ref[d:d + 1, :]
            r, ig, sp, a, mult, mult2 = _gates(u, u16, wa, ba_ref[d:d + 1, :], wi, bi_ref[d:d + 1, :], lam_d)
            a_s[...] = _shift_rows(a, 1 if d == 1 else -1)
            gate_values.append((wa, wi, lam_d, r, ig, sp, a, mult, mult2))
        _scans([(a_s0, dh_s, g_s0, True), (a_s1, dh_s, g_s1, False)])
        du = jnp.zeros((T, CG), F32)
        for d, g_s in enumerate((g_s0, g_s1)):
            reverse = d == 1
            wa, wi, lam_d, r, ig, sp, a, mult, mult2 = gate_values[d]
            g = g_s[...]
            h_prev = _shift_rows(hb_ref[...], -1) if reverse else _shift_rows(hf_ref[...], 1)
            da = g * h_prev
            dmult = g * (ig * u)
            dig = g * mult * u
            du = du + g * mult * ig
            dmult_dlog = jnp.where(mult2 > 0.0, -(a * a) * lax.rsqrt(mult2), 0.0)
            dlog_a = da * a + dmult * dmult_dlog
            dr = dlog_a * ((-LRU_C) * sp)
            dsp = jnp.sum(dlog_a * ((-LRU_C) * r), axis=0, keepdims=True)
            dlam_ref[d:d + 1, :] = dsp * (-_sigmoid(-lam_d))
            dga = dr * r * (1.0 - r)
            dgi = dig * ig * (1.0 - ig)
            dga16 = dga.astype(BF16)
            dgi16 = dgi.astype(BF16)
            du = du + _dot_nt(dga16, wa) + _dot_nt(dgi16, wi)
            dwa_ref[d, 0] = _dot_exact(jnp.where(same_half, _dot_tn(u16, dga16), 0.0), dup_t)
            dwi_ref[d, 0] = _dot_exact(jnp.where(same_half, _dot_tn(u16, dgi16), 0.0), dup_t)
            dba_ref[d:d + 1, :] = jnp.sum(dga, axis=0, keepdims=True)
            dbi_ref[d:d + 1, :] = jnp.sum(dgi, axis=0, keepdims=True)
        dcb_ref[...] = jnp.sum(du, axis=0, keepdims=True)
        for j in range(4):
            dcw_ref[j:j + 1, :] = jnp.sum(du * taps[j], axis=0, keepdims=True)
        dup_in = (_shift_rows(du, -2) * cw_ref[0:1, :] + _shift_rows(du, -1) * cw_ref[1:2, :]
                  + du * cw_ref[2:3, :] + _shift_rows(du, 1) * cw_ref[3:4, :])
        duy_ref[0] = dup_in.astype(BF16)

    wshape = jax.ShapeDtypeStruct((2, N_CG, CG, REC_BLOCK), F32)
    vec = lambda rows: jax.ShapeDtypeStruct((rows, D_REC), F32)
    dup_np = _dup_table()
    return pl.pallas_call(
        body, name="rec_bwd",
        out_shape=(jax.ShapeDtypeStruct((2, T, D_REC), BF16),
                   vec(4), vec(1), wshape, vec(2), wshape, vec(2), vec(2)),
        grid=(N_CG,),
        in_specs=[tok(0), tok(N_CG), tok(0), tok(0), tok(0),
                  per_ch(4), per_ch(1), wspec, per_ch(2), wspec, per_ch(2), per_ch(2),
                  const((REC_BLOCK, CG)), const((CG, REC_BLOCK)), const((CG, CG))],
        out_specs=(pl.BlockSpec((2, T, CG), lambda g: (0, 0, g)),
                   per_ch(4), per_ch(1), wspec, per_ch(2), wspec, per_ch(2), per_ch(2)),
        scratch_shapes=[pltpu.VMEM((T, CG), F32)] * 5,
        compiler_params=_params(dimension_semantics=("parallel",)),
    )(uy, uy, hf, hb, dyrec, conv_w, conv_b, w_a, b_a, w_i, b_i, lam,
      jnp.asarray(dup_np, BF16), jnp.asarray(dup_np.T.copy()), jnp.asarray(_pair_mask()))


TM_MIX = 256


def _mix_specs():
    tok = lambda width, blk=0: pl.BlockSpec((TM_MIX, width), lambda i: (i, blk))
    full = lambda shape: pl.BlockSpec(shape, lambda i: (0, 0))
    return tok, full


def _mix_fwd(x, att, yrec, gg, w_att_o_t, w_rec_o, w_out):
    tok, full = _mix_specs()

    def body(x_ref, att_ref, yr_ref, ga_ref, gr_ref, wao_ref, wro_ref, wo_ref, x1_ref, mixed_ref):
        y_att = _dot_nt(att_ref[...], wao_ref[...])
        y_rec = _dot(yr_ref[...], wro_ref[...])
        mixed = (_sigmoid(ga_ref[...]) * y_att + _sigmoid(gr_ref[...]) * y_rec).astype(BF16)
        mixed_ref[...] = mixed
        x1_ref[...] = x_ref[...] + _dot(mixed, wo_ref[...])

    return pl.pallas_call(
        body, name="mix_fwd",
        out_shape=(jax.ShapeDtypeStruct((T, D), F32), jax.ShapeDtypeStruct((T, D), BF16)),
        grid=(T // TM_MIX,),
        in_specs=[tok(D), tok(D_ATT), tok(D_REC), tok(D, 0), tok(D, 1),
                  full((D, D_ATT)), full((D_REC, D)), full((D, D))],
        out_specs=(tok(D), tok(D)),
        compiler_params=_params(dimension_semantics=("parallel",)),
    )(x, att, yrec, gg, gg, w_att_o_t, w_rec_o, w_out)


def _mix_bwd(dx1, att, yrec, gg, w_att_o_t, w_rec_o, w_out):
    tok, full = _mix_specs()

    def body(dx_ref, att_ref, yr_ref, ga_ref, gr_ref, wao_ref, wro_ref, wo_ref,
             dgg_ref, dya_ref, dyr_ref, datt_ref, dyrp_ref):
        dmixed = _dot_nt(dx_ref[...].astype(BF16), wo_ref[...])
        y_att = _dot_nt(att_ref[...], wao_ref[...])
        y_rec = _dot(yr_ref[...], wro_ref[...])
        sa = _sigmoid(ga_ref[...])
        sr = _sigmoid(gr_ref[...])
        dgg_ref[0] = (dmixed * y_att * sa * (1.0 - sa)).astype(BF16)
        dgg_ref[1] = (dmixed * y_rec * sr * (1.0 - sr)).astype(BF16)
        dya = (dmixed * sa).astype(BF16)
        dyr = (dmixed * sr).astype(BF16)
        dya_ref[...] = dya
        dyr_ref[...] = dyr
        datt_ref[...] = _dot(dya, wao_ref[...]).astype(BF16)
        dyrp_ref[...] = _dot_nt(dyr, wro_ref[...])

    return pl.pallas_call(
        body, name="mix_bwd",
        out_shape=(jax.ShapeDtypeStruct((2, T, D), BF16),
                   jax.ShapeDtypeStruct((T, D), BF16), jax.ShapeDtypeStruct((T, D), BF16),
                   jax.ShapeDtypeStruct((T, D_ATT), BF16), jax.ShapeDtypeStruct((T, D_REC), F32)),
        grid=(T // TM_MIX,),
        in_specs=[tok(D), tok(D_ATT), tok(D_REC), tok(D, 0), tok(D, 1),
                  full((D, D_ATT)), full((D_REC, D)), full((D, D))],
        out_specs=(pl.BlockSpec((2, TM_MIX, D), lambda i: (0, i, 0)),
                   tok(D), tok(D), tok(D_ATT), tok(D_REC)),
        compiler_params=_params(dimension_semantics=("parallel",)),
    )(dx1, att, yrec, gg, gg, w_att_o_t, w_rec_o, w_out)


TM_FFN = 256
FF_CHUNK = 1024


def _ffn_loss(x1, target, g2, gf, w_ff1_t, w_ff2):
    n_chunks = D_FF // FF_CHUNK

    def body(x1_ref, tg_ref, g2_ref, gf_ref, w1_hbm, w2_hbm,
             loss_ref, dx1_ref, h2_ref, act_ref, dpre_ref, dx2_ref, dg2_ref, dgf_ref,
             w1, w2, relu_s):
        i = pl.program_id(0)

        @pl.when(i == 0)
        def _():
            pltpu.sync_copy(w1_hbm, w1)
            pltpu.sync_copy(w2_hbm, w2)
            loss_ref[...] = jnp.zeros_like(loss_ref)
            dg2_ref[...] = jnp.zeros_like(dg2_ref)
            dgf_ref[...] = jnp.zeros_like(dgf_ref)

        x1v = x1_ref[...]
        r2 = lax.rsqrt(jnp.mean(x1v * x1v, axis=-1, keepdims=True) + EPS)
        xh2 = x1v * r2
        h2 = (xh2 * g2_ref[...]).astype(BF16)
        h2_ref[...] = h2
        x2 = x1v
        for c in range(n_chunks):
            ff = slice(c * FF_CHUNK, (c + 1) * FF_CHUNK)
            rl = jnp.maximum(_dot_nt(h2, w1[ff, :]), 0.0)
            relu_s[:, ff] = rl
            act = (rl * rl).astype(BF16)
            act_ref[:, ff] = act
            x2 = x2 + _dot(act, w2[ff, :])
        r3 = lax.rsqrt(jnp.mean(x2 * x2, axis=-1, keepdims=True) + EPS)
        xh3 = x2 * r3
        err = xh3 * gf_ref[...] - tg_ref[...]
        loss_ref[...] += 0.5 * jnp.sum(jnp.mean(err * err, axis=-1, keepdims=True))
        dy = err * (1.0 / D)
        dgf_ref[...] += jnp.sum(dy * xh3, axis=0, keepdims=True)
        dx2 = _rms_bwd(dy, xh3, r3, gf_ref[...])
        dx2_16 = dx2.astype(BF16)
        dx2_ref[...] = dx2_16
        dh2 = jnp.zeros((TM_FFN, D), F32)
        for c in range(n_chunks):
            ff = slice(c * FF_CHUNK, (c + 1) * FF_CHUNK)
            dpre = (_dot_nt(dx2_16, w2[ff, :]) * (2.0 * relu_s[:, ff])).astype(BF16)
            dpre_ref[:, ff] = dpre
            dh2 = dh2 + _dot(dpre, w1[ff, :])
        dg2_ref[...] += jnp.sum(dh2 * xh2, axis=0, keepdims=True)
        dx1_ref[...] = dx2 + _rms_bwd(dh2, xh2, r2, g2_ref[...])

    tok = lambda width: pl.BlockSpec((TM_FFN, width), lambda i: (i, 0))
    vec = pl.BlockSpec((1, D), lambda i: (0, 0))
    hbm = pl.BlockSpec(memory_space=pl.ANY)
    return pl.pallas_call(
        body, name="ffn_loss",
        out_shape=(jax.ShapeDtypeStruct((8, 128), F32), jax.ShapeDtypeStruct((T, D), F32),
                   jax.ShapeDtypeStruct((T, D), BF16), jax.ShapeDtypeStruct((T, D_FF), BF16),
                   jax.ShapeDtypeStruct((T, D_FF), BF16), jax.ShapeDtypeStruct((T, D), BF16),
                   jax.ShapeDtypeStruct((1, D), F32), jax.ShapeDtypeStruct((1, D), F32)),
        grid=(T // TM_FFN,),
        in_specs=[tok(D), tok(D), vec, vec, hbm, hbm],
        out_specs=(pl.BlockSpec((8, 128), lambda i: (0, 0)), tok(D), tok(D), tok(D_FF), tok(D_FF), tok(D),
                   vec, vec),
        scratch_shapes=[pltpu.VMEM((D_FF, D), BF16), pltpu.VMEM((D_FF, D), BF16),
                        pltpu.VMEM((TM_FFN, D_FF), F32)],
        compiler_params=_params(dimension_semantics=("arbitrary",)),
    )(x1, target, g2, gf, w_ff1_t, w_ff2)


def _local_step(x, target, p, late_weights, reduce_early):
    bias = _rpb_rows(p["rpb"])
    pairs = lambda w: w.reshape(2, N_CG, CG, REC_BLOCK)
    w_a, w_i = pairs(p["w_rg_a"]), pairs(p["w_rg_i"])
    rec_params = (p["conv_w"], p["conv_b"], w_a, p["b_rg_a"], w_i, p["b_rg_i"], p["lru_lambda"])

    qkv, uy, gg, h = _in_proj(x, p["ln1_g"], p["w_in_t"], p["b_in"])
    att = _att_fwd(qkv, bias)
    hf, hb, yrec = _rec_fwd(uy, *rec_params)
    p = {**p, **late_weights(yrec, 0)}
    x1, mixed = _mix_fwd(x, att, yrec, gg, p["w_att_o_t"], p["w_rec_o"], p["w_out"])
    p = {**p, **late_weights(x1, 1)}
    loss8, dx1, h2, act, dpre, dx2, g_ln2, g_lnf = _ffn_loss(
        x1, target, p["ln2_g"], p["lnf_g"], p["w_ff1_t"], p["w_ff2"])

    dgg, dya, dyr, datt, dyrp = _mix_bwd(dx1, att, yrec, gg, p["w_att_o_t"], p["w_rec_o"], p["w_out"])
    duy, g_cw, g_cb, g_wa, g_ba, g_wi, g_bi, g_lam = _rec_bwd(uy, hf, hb, dyrp, *rec_params)
    blocks = lambda g: g.reshape(2, N_REC_BLOCKS, REC_BLOCK, REC_BLOCK)
    grads = {
        "w_att_o_t": _matmul(dya, att, "tn", BF16, "g_w_att_o"),
        "conv_w": g_cw, "conv_b": g_cb, "w_rg_a": blocks(g_wa), "b_rg_a": g_ba,
        "w_rg_i": blocks(g_wi), "b_rg_i": g_bi, "lru_lambda": g_lam,
        "w_rec_o": _matmul(yrec, dyr, "tn", BF16, "g_w_rec_o"),
        "w_out": _matmul(mixed, dx1, "tn", BF16, "g_w_out"),
        "ln2_g": g_ln2,
        "w_ff1_t": _matmul(dpre, h2, "tn", BF16, "g_w_ff1"),
        "w_ff2": _matmul(act, dx2, "tn", BF16, "g_w_ff2"),
        "lnf_g": g_lnf,
    }
    after = reduce_early(grads)
    dqkv, gbias = _att_bwd(qkv, bias, datt, after)
    dz = (dqkv, duy, dgg)
    grad_x, g_ln1 = _dh_norm1_bwd(dz, p["w_in_t"], x, p["ln1_g"], dx1)
    g_w_in_t, g_b_in = _grad_w_in(dz, h)
    grads.update(ln1_g=g_ln1, w_in_t=g_w_in_t, b_in=g_b_in, rpb=_rpb_fold(gbias))
    return loss8[0:1, 0:1], grad_x, grads


MESH_ID = pl.DeviceIdType.MESH
ANY = pl.BlockSpec(memory_space=pl.ANY)

CHAN_BLOCK_ROWS = 32
GATE_ROWS = 2 * 2 * N_REC_BLOCKS * REC_BLOCK * REC_BLOCK // (N_DEV * D)
SECTIONS = (("w_in_t", 704, D), ("w_rec_o", 128, D), ("w_out", 128, D), ("w_ff1_t", 512, D),
            ("w_ff2", 512, D), ("chan", CHAN_BLOCK_ROWS, D), ("w_att_o_t", 128, D_ATT),
            ("gates", GATE_ROWS, D))
N_SEC = len(SECTIONS)
N_CHAN_ROWS = 10
CHAN = (("conv_w", 4), ("b_rg_a", 2), ("b_rg_i", 2), ("lru_lambda", 2))


def _position():
    return lax.axis_index("x"), lax.axis_index("y"), lax.axis_index("c")


def _other_chips(x, y):
    return [(1 - x, y), (x, 1 - y), (1 - x, 1 - y)]


def _block_of(ref, dev, rows):
    return ref.at[pl.ds(pl.multiple_of(dev * rows, 16), rows)]


def _all_gather(shards, name):
    ns = len(shards)

    def body(*refs):
        x_refs, out_refs, done_ref = refs[:ns], refs[ns:2 * ns], refs[2 * ns]
        send_sems, recv_sems, local_sems = refs[2 * ns + 1:]
        done_ref[0, 0] = 0.0
        x, y, c = _position()
        me, sibling = (x, y, c), (x, y, 1 - c)
        x_nbr, y_nbr, diagonal = _other_chips(x, y)
        north = c == 1
        relay_from = (jnp.where(north, x_nbr[0], y_nbr[0]), jnp.where(north, x_nbr[1], y_nbr[1]))
        relay_to = (jnp.where(north, y_nbr[0], x_nbr[0]), jnp.where(north, y_nbr[1], x_nbr[1]))

        def rows(s, px, py, pc):
            return _block_of(out_refs[s], 4 * px + 2 * py + pc, shards[s].shape[0])

        def copy(k, s, block, to, from_shard=False):
            return pltpu.make_async_remote_copy(
                src_ref=x_refs[s] if from_shard else rows(s, *block), dst_ref=rows(s, *block),
                send_sem=send_sems.at[k * ns + s], recv_sem=recv_sems.at[k * ns + s],
                device_id=to, device_id_type=MESH_ID)

        sections = range(ns)
        mine = [pltpu.make_async_copy(x_refs[s], rows(s, *me), local_sems.at[s]) for s in sections]
        sent = [copy(k, s, me, to, True) for k, to in enumerate((sibling, (*x_nbr, c), (*y_nbr, c)))
                for s in sections]
        for cp in mine + sent:
            cp.start()
        for s in sections:
            copy(1, s, (*x_nbr, c), me).wait_recv()
            copy(2, s, (*y_nbr, c), me).wait_recv()
            sent += [copy(3, s, (*relay_from, c), (*relay_to, c)),
                     copy(4, s, (*x_nbr, c), sibling), copy(5, s, (*y_nbr, c), sibling)]
            for cp in sent[-3:]:
                cp.start()
        for s in sections:
            copy(3, s, (*diagonal, c), me).wait_recv()
            sent.append(copy(6, s, (*diagonal, c), sibling))
            sent[-1].start()
        for s in sections:
            copy(0, s, sibling, me).wait_recv()
            for k, chip in ((4, x_nbr), (5, y_nbr), (6, diagonal)):
                copy(k, s, (*chip, 1 - c), me).wait_recv()
        for cp in sent:
            cp.wait_send()
        for cp in mine:
            cp.wait()

    return pl.pallas_call(
        body, name=name,
        out_shape=tuple(jax.ShapeDtypeStruct((N_DEV * s.shape[0], s.shape[1]), s.dtype) for s in shards)
        + (jax.ShapeDtypeStruct((1, 1), F32),),
        in_specs=[ANY] * ns,
        out_specs=(ANY,) * ns + (pl.BlockSpec(memory_space=pltpu.SMEM),),
        scratch_shapes=[pltpu.SemaphoreType.DMA((7 * ns,)), pltpu.SemaphoreType.DMA((7 * ns,)),
                        pltpu.SemaphoreType.DMA((ns,))],
    )(*shards)


HBM = pl.BlockSpec(memory_space=pltpu.HBM)
SEM = pl.BlockSpec(memory_space=pltpu.SEMAPHORE)
EFFECT = pltpu.SideEffectType.DATAFLOW_SIDE_EFFECTING


def _in_hbm(a):
    return pltpu.with_memory_space_constraint(a, pltpu.HBM)


def _first_hop_copies(shards, x_refs, zones, send_sems, recv_sems):
    ns = len(shards)
    x, y, c = _position()
    targets = [(x, y, 1 - c)] + [(cx, cy, c) for cx, cy in _other_chips(x, y)]
    return [pltpu.make_async_remote_copy(
        src_ref=x_refs[s], dst_ref=_block_of(zones[s], 4 * x + 2 * y + c, shards[s].shape[0]),
        send_sem=send_sems.at[k * ns + s], recv_sem=recv_sems.at[k * ns + s],
        device_id=to, device_id_type=MESH_ID)
        for k, to in enumerate(targets) for s in range(ns)]


def _after_all(arrays, name):
    def body(*refs):
        refs[-1][...] = jnp.zeros_like(refs[-1])

    return pl.pallas_call(
        body, name=name,
        out_shape=jax.ShapeDtypeStruct((8, LANES), F32),
        in_specs=[pl.BlockSpec(memory_space=pl.ANY)] * len(arrays),
        out_specs=pl.BlockSpec(memory_space=pltpu.VMEM),
    )(*arrays)


def _own_blocks_placed(shards, after):
    ns = len(shards)
    x, y, c = _position()
    me = jnp.reshape(4 * x + 2 * y + c, (1,)).astype(jnp.int32)
    shards = [*shards[:-1], shards[-1] + after.astype(shards[-1].dtype)]

    def body(me_ref, *refs):
        for s in range(ns):
            refs[ns + s][...] = refs[s][...]

    return pl.pallas_call(
        body, name="own_blocks_placed",
        out_shape=tuple(jax.ShapeDtypeStruct((N_DEV * s.shape[0], s.shape[1]), s.dtype) for s in shards),
        grid_spec=pltpu.PrefetchScalarGridSpec(
            num_scalar_prefetch=1, grid=(1,),
            in_specs=[pl.BlockSpec(s.shape, lambda i, me: (0, 0)) for s in shards],
            out_specs=tuple(pl.BlockSpec(s.shape, lambda i, me: (me[0], 0)) for s in shards)),
        compiler_params=_params(dimension_semantics=("arbitrary",)),
    )(me, *shards)


def _gather_start(shards, after, name):
    ns = len(shards)
    zones = _own_blocks_placed(shards, after)

    def body(*refs):
        for cp in _first_hop_copies(shards, refs[:ns], refs[ns:2 * ns], refs[2 * ns], refs[2 * ns + 1]):
            cp.start()
        refs[-1][...] = jnp.zeros_like(refs[-1])

    out = pl.pallas_call(
        body, name=name,
        out_shape=(pltpu.SemaphoreType.DMA((4 * ns,)), pltpu.SemaphoreType.DMA((4 * ns,)),
                   *[pltpu.HBM(a.shape, a.dtype) for a in (*shards, *zones)],
                   jax.ShapeDtypeStruct((8, LANES), F32)),
        in_specs=[HBM] * (2 * ns),
        out_specs=(SEM, SEM, *[HBM] * (2 * ns), pl.BlockSpec(memory_space=pltpu.VMEM)),
        input_output_aliases={i: 2 + i for i in range(2 * ns)},
        compiler_params=pltpu.CompilerParams(has_side_effects=EFFECT),
    )(*[_in_hbm(a) for a in shards], *[_in_hbm(a) for a in zones])
    return out[0], out[1], out[2:2 + ns], out[2 + ns:2 + 2 * ns], out[-1]


def _gather_wait(send_sems, recv_sems, shards, zones, which, after, name):
    ns = len(shards)

    def body(*refs):
        copies = _first_hop_copies(shards, refs[:ns], refs[ns:2 * ns], refs[2 * ns], refs[2 * ns + 1])
        for i, cp in enumerate(copies):
            if i % ns in which:
                cp.wait_send()
                cp.wait_recv()

    out = pl.pallas_call(
        body, name=name,
        out_shape=tuple(pltpu.HBM(a.shape, a.dtype) for a in (*shards, *zones)),
        in_specs=[HBM] * (2 * ns) + [SEM, SEM, ANY],
        out_specs=(HBM,) * (2 * ns),
        input_output_aliases={i: i for i in range(2 * ns)},
        compiler_params=pltpu.CompilerParams(has_side_effects=EFFECT),
    )(*shards, *zones, send_sems, recv_sems, after)
    return out[:ns], out[ns:]


def _gather_pass_on(rows, zones, name):
    ns = len(zones)

    def body(*refs):
        in_refs, out_refs = refs[:ns], refs[ns:2 * ns]
        send_sems, recv_sems = refs[2 * ns:]
        x, y, c = _position()
        copies = [pltpu.make_async_remote_copy(
            src_ref=_block_of(in_refs[s], 4 * cx + 2 * cy + c, rows[s]),
            dst_ref=_block_of(out_refs[s], 4 * cx + 2 * cy + c, rows[s]),
            send_sem=send_sems.at[j * ns + s], recv_sem=recv_sems.at[j * ns + s],
            device_id=(x, y, 1 - c), device_id_type=MESH_ID)
            for j, (cx, cy) in enumerate(_other_chips(x, y)) for s in range(ns)]
        for cp in copies:
            cp.start()
        for cp in copies:
            cp.wait_recv()
        for cp in copies:
            cp.wait_send()

    return pl.pallas_call(
        body, name=name,
        out_shape=tuple(jax.ShapeDtypeStruct(z.shape, z.dtype) for z in zones),
        in_specs=[ANY] * ns, out_specs=(ANY,) * ns,
        input_output_aliases={i: i for i in range(ns)},
        scratch_shapes=[pltpu.SemaphoreType.DMA((3 * ns,)), pltpu.SemaphoreType.DMA((3 * ns,))],
    )(*zones)


def _pair_exchange(sections, grads, name):
    ns = len(sections)

    def body(*refs):
        g_refs, land = refs[:ns], refs[ns:2 * ns]
        send_sems, recv_sems = refs[2 * ns:]
        x, y, c = _position()
        copies = [pltpu.make_async_remote_copy(
            src_ref=_block_of(g_refs[s], 2 * k + 1 - c, rows), dst_ref=land[s].at[k],
            send_sem=send_sems.at[k * ns + s], recv_sem=recv_sems.at[k * ns + s],
            device_id=(x, y, 1 - c), device_id_type=MESH_ID)
            for k in range(N_CHIPS) for s, (_, rows, _) in enumerate(sections)]
        for cp in copies:
            cp.start()
        for cp in copies:
            cp.wait_recv()
        for cp in copies:
            cp.wait_send()

    n = N_CHIPS * ns
    return pl.pallas_call(
        body, name=name,
        out_shape=tuple(jax.ShapeDtypeStruct((N_CHIPS, rows, cols), BF16) for _, rows, cols in sections),
        in_specs=[ANY] * ns, out_specs=(ANY,) * ns,
        scratch_shapes=[pltpu.SemaphoreType.DMA((n,)), pltpu.SemaphoreType.DMA((n,))],
    )(*grads)


def _pair_add(sections, grads, got, core, name):
    ns = len(sections)

    def body(core_ref, *refs):
        g_refs, got_refs, p_refs = refs[:ns], refs[ns:2 * ns], refs[2 * ns:]
        for s in range(ns):
            p_refs[s][0] = (g_refs[s][...].astype(F32) + got_refs[s][0].astype(F32)).astype(BF16)

    slot = [pl.BlockSpec((1, rows, cols), lambda k, c: (k, 0, 0)) for _, rows, cols in sections]
    return pl.pallas_call(
        body, name=name,
        out_shape=tuple(jax.ShapeDtypeStruct((N_CHIPS, rows, cols), BF16) for _, rows, cols in sections),
        grid_spec=pltpu.PrefetchScalarGridSpec(
            num_scalar_prefetch=1, grid=(N_CHIPS,),
            in_specs=[pl.BlockSpec((rows, cols), lambda k, c: (2 * k + c[0], 0)) for _, rows, cols in sections]
            + slot,
            out_specs=tuple(slot)),
        compiler_params=_params(dimension_semantics=("parallel",)),
    )(core, *grads, *got)


def _chip_copies(sections, p_refs, land, send_sems, recv_sems):
    ns = len(sections)
    x, y, c = _position()
    return [pltpu.make_async_remote_copy(
        src_ref=p_refs[s].at[2 * cx + cy], dst_ref=land[s].at[j],
        send_sem=send_sems.at[j * ns + s], recv_sem=recv_sems.at[j * ns + s],
        device_id=(cx, cy, c), device_id_type=MESH_ID)
        for j, (cx, cy) in enumerate(_other_chips(x, y)) for s in range(ns)]


def _chip_exchange(sections, parts, name):
    ns = len(sections)

    def body(*refs):
        copies = _chip_copies(sections, refs[:ns], refs[ns:2 * ns], *refs[2 * ns:])
        for cp in copies:
            cp.start()
        for cp in copies:
            cp.wait_recv()
        for cp in copies:
            cp.wait_send()

    n = 3 * ns
    return pl.pallas_call(
        body, name=name,
        out_shape=tuple(jax.ShapeDtypeStruct((3, rows, cols), BF16) for _, rows, cols in sections),
        in_specs=[ANY] * ns, out_specs=(ANY,) * ns,
        scratch_shapes=[pltpu.SemaphoreType.DMA((n,)), pltpu.SemaphoreType.DMA((n,))],
    )(*parts)


def _chip_exchange_start(sections, parts, name):
    ns = len(sections)

    def body(*refs):
        p_refs, land = refs[:ns], refs[ns:2 * ns]
        send_sems, recv_sems = refs[2 * ns], refs[2 * ns + 1]
        token = refs[-1]
        for cp in _chip_copies(sections, p_refs, land, send_sems, recv_sems):
            cp.start()
        token[...] = jnp.zeros_like(token)

    zones = [lax.empty((3, rows, cols), BF16) for _, rows, cols in sections]
    out = pl.pallas_call(
        body, name=name,
        out_shape=(pltpu.SemaphoreType.DMA((3 * ns,)), pltpu.SemaphoreType.DMA((3 * ns,)),
                   *[pltpu.HBM(a.shape, a.dtype) for a in parts], *[pltpu.HBM(a.shape, a.dtype) for a in zones],
                   jax.ShapeDtypeStruct((8, LANES), F32)),
        in_specs=[HBM] * (2 * ns),
        out_specs=(SEM, SEM, *[HBM] * (2 * ns), pl.BlockSpec(memory_space=pltpu.VMEM)),
        input_output_aliases={i: 2 + i for i in range(2 * ns)},
        compiler_params=pltpu.CompilerParams(has_side_effects=EFFECT),
    )(*[_in_hbm(a) for a in parts], *[_in_hbm(a) for a in zones])
    return out[0], out[1], out[2:2 + ns], out[2 + ns:2 + 2 * ns], out[-1]


def _chip_exchange_wait(sections, send_sems, recv_sems, parts, zones, after, name):
    ns = len(sections)

    def body(*refs):
        p_refs, land = refs[:ns], refs[ns:2 * ns]
        for cp in _chip_copies(sections, p_refs, land, refs[2 * ns], refs[2 * ns + 1]):
            cp.wait_send()
            cp.wait_recv()

    out = pl.pallas_call(
        body, name=name,
        out_shape=tuple(pltpu.HBM(a.shape, a.dtype) for a in (*parts, *zones)),
        in_specs=[HBM] * (2 * ns) + [SEM, SEM, ANY],
        out_specs=(HBM,) * (2 * ns),
        input_output_aliases={i: i for i in range(2 * ns)},
        compiler_params=pltpu.CompilerParams(has_side_effects=EFFECT),
    )(*parts, *zones, send_sems, recv_sems, after)
    return out[:ns], out[ns:]


def _grad_finish(sections, parts, far, chip, name):
    ns = len(sections)

    def body(chip_ref, *refs):
        p_refs, b_refs, g_refs = refs[:ns], refs[ns:2 * ns], refs[2 * ns:]
        for s in range(ns):
            g = p_refs[s][0].astype(F32)
            for j in range(3):
                g = g + b_refs[s][j].astype(F32)
            g_refs[s][...] = g

    half = [(rows // 2, cols) for _, rows, cols in sections]
    return pl.pallas_call(
        body, name=name,
        out_shape=tuple(jax.ShapeDtypeStruct((rows, cols), F32) for _, rows, cols in sections),
        grid_spec=pltpu.PrefetchScalarGridSpec(
            num_scalar_prefetch=1, grid=(2,),
            in_specs=[pl.BlockSpec((1, r, c), lambda i, chip: (chip[0], i, 0)) for r, c in half]
            + [pl.BlockSpec((3, r, c), lambda i, chip: (0, i, 0)) for r, c in half],
            out_specs=tuple(pl.BlockSpec((r, c), lambda i, chip: (i, 0)) for r, c in half)),
        compiler_params=_params(dimension_semantics=("parallel",)),
    )(chip, *parts, *far)


def _sum_devices(parts, rows, name):
    cols = parts.shape[1]
    tr = rows // 2

    def body(*refs):
        s = refs[0][...].astype(F32)
        for d in range(1, N_DEV):
            s = s + refs[d][...].astype(F32)
        refs[N_DEV][...] = s

    return pl.pallas_call(
        body, name=name,
        out_shape=jax.ShapeDtypeStruct((rows, cols), F32),
        grid=(2,),
        in_specs=[pl.BlockSpec((tr, cols), lambda i, d=d: (2 * d + i, 0)) for d in range(N_DEV)],
        out_specs=pl.BlockSpec((tr, cols), lambda i: (i, 0)),
        compiler_params=_params(dimension_semantics=("parallel",)),
    )(*([parts] * N_DEV))


def _adamw_step(w_ref, g_ref, m_ref, v_ref, d_ref, nm_ref, nv_ref):
    c1 = 1.0 / (1.0 - ADAM_B1 ** ADAM_STEP)
    c2 = 1.0 / (1.0 - ADAM_B2 ** ADAM_STEP)
    gv = g_ref[...]
    nm = ADAM_B1 * m_ref[...] + (1.0 - ADAM_B1) * gv
    nv = ADAM_B2 * v_ref[...] + (1.0 - ADAM_B2) * (gv * gv)
    nm_ref[...] = nm
    nv_ref[...] = nv
    d_ref[...] = (-ADAM_LR) * ((nm * c1) / (jnp.sqrt(nv * c2) + ADAM_EPS) + ADAM_WD * w_ref[...])


def _adamw_small(params, name):
    n = len(params)

    def body(*refs):
        for k in range(n):
            _adamw_step(*refs[4 * k:4 * k + 4], *refs[4 * n + 3 * k:4 * n + 3 * k + 3])

    out = pl.pallas_call(
        body, name=name,
        out_shape=tuple(jax.ShapeDtypeStruct(p[0].shape, F32) for p in params for _ in range(3)),
    )(*[a for p in params for a in p])
    return [out[3 * k:3 * k + 3] for k in range(n)]


def _adamw(w, g, m, v, name):
    rows, cols = w.shape
    tr = rows
    while tr * cols * 4 > (1 << 20) and tr % 16 == 0:
        tr //= 2

    def body(*refs):
        _adamw_step(*refs)

    spec = pl.BlockSpec((tr, cols), lambda i: (i, 0))
    shape = jax.ShapeDtypeStruct((rows, cols), F32)
    return pl.pallas_call(
        body, name=name,
        out_shape=(shape, shape, shape),
        grid=(rows // tr,),
        in_specs=[spec] * 4, out_specs=(spec,) * 3,
        compiler_params=_params(dimension_semantics=("parallel",)),
    )(w, g, m, v)


NAMES = ("ln1_g", "w_in", "b_in", "rpb", "w_att_o", "conv_w", "conv_b", "w_rg_a", "b_rg_a", "w_rg_i",
         "b_rg_i", "lru_lambda", "w_rec_o", "w_out", "ln2_g", "w_ff1", "w_ff2", "lnf_g")
TRANSPOSED = {"w_in": "w_in_t", "w_att_o": "w_att_o_t", "w_ff1": "w_ff1_t"}
ROW_SHARDED = ("w_rec_o", "w_out", "w_ff2")
REPLICATED = (("ln1_g", (1, D)), ("b_in", (1, D_IN)), ("rpb", (N_HEADS * N_RPB_R, N_RPB_C)),
              ("conv_b", (1, D_REC)), ("w_rg_a", (2 * N_REC_BLOCKS * REC_BLOCK, REC_BLOCK)),
              ("w_rg_i", (2 * N_REC_BLOCKS * REC_BLOCK, REC_BLOCK)), ("ln2_g", (1, D)), ("lnf_g", (1, D)))
GATE_BLOCKS = ("w_rg_a", "w_rg_i")
SMALL_ROWS = 112


def _chan_bits(vectors):
    chan = jnp.concatenate(vectors, axis=0)
    bits = lax.bitcast_convert_type(chan, BF16).reshape(-1)
    return jnp.pad(bits, (0, CHAN_BLOCK_ROWS * D - bits.shape[0])).reshape(CHAN_BLOCK_ROWS, D)


def _chan_from_bits(gathered):
    bits = gathered.reshape(N_DEV, CHAN_BLOCK_ROWS * D)[:, :2 * N_CHAN_ROWS * LANES]
    chan = lax.bitcast_convert_type(bits.reshape(N_DEV, N_CHAN_ROWS, LANES, 2), F32)
    return chan.transpose(1, 0, 2).reshape(N_CHAN_ROWS, D)


def kernel(x, ln1_g, w_in, b_in, rpb, w_att_o, conv_w, conv_b, w_rg_a, b_rg_a, w_rg_i, b_rg_i, lru_lambda, w_rec_o, w_out, ln2_g, w_ff1, w_ff2, lnf_g, loss_target, m_ln1_g, m_w_in, m_b_in, m_rpb, m_w_att_o, m_conv_w, m_conv_b, m_w_rg_a, m_b_rg_a, m_w_rg_i, m_b_rg_i, m_lru_lambda, m_w_rec_o, m_w_out, m_ln2_g, m_w_ff1, m_w_ff2, m_lnf_g, v_ln1_g, v_w_in, v_b_in, v_rpb, v_w_att_o, v_conv_w, v_conv_b, v_w_rg_a, v_b_rg_a, v_w_rg_i, v_b_rg_i, v_lru_lambda, v_w_rec_o, v_w_out, v_ln2_g, v_w_ff1, v_w_ff2, v_lnf_g):
    w = dict(zip(NAMES, (ln1_g, w_in, b_in, rpb, w_att_o, conv_w, conv_b, w_rg_a, b_rg_a, w_rg_i,
                         b_rg_i, lru_lambda, w_rec_o, w_out, ln2_g, w_ff1, w_ff2, lnf_g)))
    m = dict(zip(NAMES, (m_ln1_g, m_w_in, m_b_in, m_rpb, m_w_att_o, m_conv_w, m_conv_b, m_w_rg_a,
                         m_b_rg_a, m_w_rg_i, m_b_rg_i, m_lru_lambda, m_w_rec_o, m_w_out, m_ln2_g,
                         m_w_ff1, m_w_ff2, m_lnf_g)))
    v = dict(zip(NAMES, (v_ln1_g, v_w_in, v_b_in, v_rpb, v_w_att_o, v_conv_w, v_conv_b, v_w_rg_a,
                         v_b_rg_a, v_w_rg_i, v_b_rg_i, v_lru_lambda, v_w_rec_o, v_w_out, v_ln2_g,
                         v_w_ff1, v_w_ff2, v_lnf_g)))
    xi, yi, ci = _position()

    shard = {t: w[n][0].T.astype(BF16) for n, t in TRANSPOSED.items()}
    shard.update({n: w[n][0].astype(BF16) for n in ROW_SHARDED})
    shard["chan"] = _chan_bits([w[n][0] for n, _ in CHAN])
    first, later = ("w_in_t", "chan"), ("w_rec_o", "w_out", "w_att_o_t", "w_ff1_t", "w_ff2")
    for_merge = ("w_rec_o", "w_out", "w_att_o_t")
    *gathered, done = _all_gather([shard[n] for n in first], "weight_all_gather")
    p = dict(zip(first, gathered))
    send_sems, recv_sems, sent, zones, token = _gather_start([shard[n] for n in later], done,
                                                             "weight_gather_start")

    travelling = {"shards": sent, "zones": zones}

    def late_weights(after, stage):
        which = [i for i, n in enumerate(later) if (n in for_merge) == (stage == 0)]
        travelling["shards"], travelling["zones"] = _gather_wait(
            send_sems, recv_sems, travelling["shards"], travelling["zones"], which, after,
            "weight_gather_wait_%d" % stage)
        names = [later[i] for i in which]
        return dict(zip(names, _gather_pass_on([shard[n].shape[0] for n in names],
                                               [travelling["zones"][i] for i in which],
                                               "weight_gather_pass_on_%d" % stage)))

    chan = _chan_from_bits(p.pop("chan"))
    r0 = 0
    for n, rows in CHAN:
        p[n] = chan[r0:r0 + rows]
        r0 += rows
    p.update(ln1_g=w["ln1_g"], b_in=w["b_in"] + token[0, 0], rpb=w["rpb"][0], conv_b=w["conv_b"],
             w_rg_a=w["w_rg_a"][0], w_rg_i=w["w_rg_i"][0], ln2_g=w["ln2_g"],
             lnf_g=w["lnf_g"].reshape(1, D))

    core = jnp.reshape(ci, (1,)).astype(jnp.int32)
    chip = jnp.reshape(2 * xi + yi, (1,)).astype(jnp.int32)
    early_sections, late_sections = SECTIONS[1:], SECTIONS[:1]
    in_flight = {}

    def reduce_early(grads):
        chan_g = jnp.concatenate([grads[n] for n, _ in CHAN], axis=0)
        chan_g = chan_g.reshape(N_CHAN_ROWS, N_DEV, LANES).transpose(1, 0, 2).astype(BF16)
        chan_g = jnp.pad(chan_g.reshape(N_DEV, -1), ((0, 0), (0, CHAN_BLOCK_ROWS * D - N_CHAN_ROWS * LANES)))
        grads["chan"] = chan_g.reshape(N_DEV * CHAN_BLOCK_ROWS, D)
        grads["gates"] = jnp.concatenate([grads[n].reshape(-1, D) for n in GATE_BLOCKS], axis=0).astype(BF16)
        sect = [grads[n] for n, _, _ in early_sections]
        got = _pair_exchange(early_sections, sect, "grad_pair_exchange_early")
        parts = _pair_add(early_sections, sect, got, core, "grad_pair_add_early")
        in_flight["early"] = _chip_exchange_start(early_sections, parts, "grad_chip_exchange_start")
        return in_flight["early"][-1][0, 0]

    loss_part, grad_x, grads = _local_step(x[0], loss_target[0], p, late_weights, reduce_early)
    sect = [grads[n] for n, _, _ in late_sections]
    got = _pair_exchange(late_sections, sect, "grad_pair_exchange_late")
    late_parts = _pair_add(late_sections, sect, got, core, "grad_pair_add_late")
    in_flight["late"] = _chip_exchange_start(late_sections, late_parts, "grad_chip_exchange_start_late")

    def finish(group, sections, after, name):
        send_sems, recv_sems, parts, zones, _ = in_flight[group]
        parts, far = _chip_exchange_wait(sections, send_sems, recv_sems, parts, zones, after,
                                         "grad_chip_exchange_wait_" + name)
        return dict(zip((n for n, _, _ in sections),
                        _grad_finish(sections, parts, far, chip, "grad_finish_" + name)))

    started_late = in_flight["late"][-1]
    summed = finish("early", early_sections, started_late, "early")

    flat = jnp.concatenate([grads[n].reshape(-1) for n, _ in REPLICATED if n not in GATE_BLOCKS]
                           + [loss_part.reshape(-1) + started_late[0, 0]])
    n_small = flat.shape[0]
    flat = jnp.pad(flat, (0, SMALL_ROWS * LANES - n_small)).reshape(SMALL_ROWS, LANES)
    small_parts, gate_sum, _ = _all_gather([flat, summed["gates"]], "small_grad_all_gather")
    small = _sum_devices(small_parts, SMALL_ROWS, "small_grad_sum").reshape(-1)
    loss = small[n_small - 1]

    g, delta, new_m, new_v = {}, {}, {}, {}

    def update(n, g2, shape2):
        d2, m2, v2 = _adamw(w[n].reshape(shape2), g2, m[n].reshape(shape2), v[n].reshape(shape2),
                            "adamw_" + n)
        g[n], delta[n], new_m[n], new_v[n] = (a.reshape(w[n].shape) for a in (g2, d2, m2, v2))

    small_params = []
    o = 0
    for n, shape2 in REPLICATED:
        if n in GATE_BLOCKS:
            k, rows = GATE_BLOCKS.index(n), gate_sum.shape[0] // len(GATE_BLOCKS)
            update(n, gate_sum[k * rows:(k + 1) * rows].reshape(shape2), shape2)
        else:
            size = shape2[0] * shape2[1]
            small_params.append((n, small[o:o + size].reshape(shape2), shape2))
            o += size
    chan_back = summed["chan"].reshape(-1)[:N_CHAN_ROWS * LANES].reshape(N_CHAN_ROWS, LANES)
    r0 = 0
    for n, rows in CHAN:
        small_params.append((n, chan_back[r0:r0 + rows], (rows, LANES)))
        r0 += rows
    results = _adamw_small([(w[n].reshape(s2), g2, m[n].reshape(s2), v[n].reshape(s2))
                            for n, g2, s2 in small_params], "adamw_vectors")
    for (n, g2, _), (d2, m2, v2) in zip(small_params, results):
        g[n], delta[n], new_m[n], new_v[n] = (a.reshape(w[n].shape) for a in (g2, d2, m2, v2))

    for n in ROW_SHARDED:
        update(n, summed[n], summed[n].shape)
    for n, t in TRANSPOSED.items():
        if t in summed:
            update(n, summed[t].T, summed[t].shape[::-1])
    summed = finish("late", late_sections, _after_all(list(delta.values()), "updates_done"), "late")
    update("w_in", summed["w_in_t"].T, summed["w_in_t"].shape[::-1])

    return (loss, grad_x[None], *[g[n] for n in NAMES], *[delta[n] for n in NAMES],
            *[new_m[n] for n in NAMES], *[new_v[n] for n in NAMES])
```

```python
import math

import numpy as np
import jax
import jax.numpy as jnp
from jax import lax
from jax.experimental import pallas as pl
from jax.experimental.pallas import tpu as pltpu

F32 = jnp.float32
BF16 = jnp.bfloat16

T = 2048
D = 1024
D_ATT = 512
D_REC = 1024
D_FF = 4096
D_IN = 5632
N_HEADS = 8
DH = 64
GRID_W = 64
ROWS = T // GRID_W
WIN_H = 8
WIN_W = 16
KWIN = WIN_H * GRID_W
N_RPB_R = 2 * WIN_H - 1
N_RPB_C = 2 * WIN_W - 1
N_REC_BLOCKS = 16
REC_BLOCK = 64
CG = 128
N_CG = D_REC // CG
LRU_C = 8.0
EPS = 1e-6
N_DEV = 8
N_CHIPS = 4
LANES = 128

ADAM_LR = 0.001
ADAM_B1 = 0.9
ADAM_B2 = 0.999
ADAM_EPS = 1e-08
ADAM_WD = 0.01
ADAM_STEP = 10

MESH_AXES = ("x", "y", "c")
VMEM_LIMIT = 56 * 1024 * 1024

TILE = 512
DZ_ARRAYS = ((0, 3, 1), (3, 4, 2), (7, 4, 2))
N_DZ_TILES = D_IN // TILE


def _params(**kw):
    return pltpu.CompilerParams(vmem_limit_bytes=VMEM_LIMIT, **kw)


def _att_tables():
    rq = np.arange(2 * GRID_W) % GRID_W
    kc = np.arange(KWIN) % GRID_W
    win_start = np.clip(rq - WIN_W // 2, 0, GRID_W - WIN_W)
    valid = (kc[None, :] >= win_start[:, None]) & (kc[None, :] < win_start[:, None] + WIN_W)
    return valid.astype(np.float32), _pair_mask()


def _pair_mask():
    half = np.arange(2 * DH) // DH
    return (half[:, None] == half[None, :]).astype(np.float32)


def _dup_table():
    return np.concatenate([np.eye(REC_BLOCK, dtype=np.float32)] * 2, axis=1)


def _sigmoid(x):
    return 0.5 * jnp.tanh(0.5 * x) + 0.5


def _softplus(x):
    return jnp.maximum(x, 0.0) + jnp.log(1.0 + jnp.exp(-jnp.abs(x)))


def _one_minus_square(log_a, a):
    x = 2.0 * log_a
    series = -x * (1.0 + x * (0.5 + x * (1.0 / 6.0)))
    return jnp.where(x > -0.02, series, 1.0 - a * a)


_GELU_C = math.sqrt(2.0 / math.pi)


def _gelu_and_grad(x):
    x2 = x * x
    inner = _GELU_C * (x + 0.044715 * x * x2)
    t = jnp.tanh(inner)
    g = 0.5 * x * (1.0 + t)
    dg = 0.5 * (1.0 + t) + 0.5 * x * (1.0 - t * t) * _GELU_C * (1.0 + 3.0 * 0.044715 * x2)
    return g, dg


def _dot(a, b):
    return jnp.dot(a, b, preferred_element_type=F32)


def _dot_nt(a, b):
    return lax.dot_general(a, b, (((1,), (1,)), ((), ())), preferred_element_type=F32)


def _dot_tn(a, b):
    return lax.dot_general(a, b, (((0,), (0,)), ((), ())), preferred_element_type=F32)


def _dot_exact(a, b):
    return jnp.dot(a, b, precision=lax.Precision.HIGHEST, preferred_element_type=F32)


def _shift_rows(x, s):
    n = x.shape[0]
    rows = lax.broadcasted_iota(jnp.int32, x.shape, 0)
    y = pltpu.roll(x, s % n, 0)
    if s > 0:
        return jnp.where(rows >= s, y, 0.0)
    return jnp.where(rows < n + s, y, 0.0)


def _rms_bwd(dh, xh, r, g):
    dxh = dh * g
    return r * (dxh - xh * jnp.mean(dxh * xh, axis=-1, keepdims=True))


def _matmul(a, b, mode, out_dtype, name, tm=512, tn=1024, tk=2048):
    if mode == "nn":
        (m, k), (k2, n) = a.shape, b.shape
    elif mode == "nt":
        (m, k), (n, k2) = a.shape, b.shape
    else:
        (k, m), (k2, n) = a.shape, b.shape
    assert k == k2
    tm, tn, tk = min(tm, m), min(tn, n), min(tk, k)
    assert m % tm == 0 and n % tn == 0 and k % tk == 0
    nk = k // tk
    dot = {"nn": _dot, "nt": _dot_nt, "tn": _dot_tn}[mode]

    def body(a_ref, b_ref, o_ref, acc):
        kk = pl.program_id(2)
        part = dot(a_ref[...].astype(BF16), b_ref[...].astype(BF16))
        if nk == 1:
            o_ref[...] = part.astype(out_dtype)
            return

        @pl.when(kk == 0)
        def _():
            acc[...] = part

        @pl.when(kk > 0)
        def _():
            acc[...] += part

        @pl.when(kk == nk - 1)
        def _():
            o_ref[...] = acc[...].astype(out_dtype)

    if mode == "tn":
        a_spec = pl.BlockSpec((tk, tm), lambda i, j, kk: (kk, i))
    else:
        a_spec = pl.BlockSpec((tm, tk), lambda i, j, kk: (i, kk))
    if mode == "nt":
        b_spec = pl.BlockSpec((tn, tk), lambda i, j, kk: (j, kk))
    else:
        b_spec = pl.BlockSpec((tk, tn), lambda i, j, kk: (kk, j))
    return pl.pallas_call(
        body, name=name,
        out_shape=jax.ShapeDtypeStruct((m, n), out_dtype),
        grid=(m // tm, n // tn, nk),
        in_specs=[a_spec, b_spec],
        out_specs=pl.BlockSpec((tm, tn), lambda i, j, kk: (i, j)),
        scratch_shapes=[pltpu.VMEM((tm, tn) if nk > 1 else (8, LANES), F32)],
        compiler_params=_params(dimension_semantics=("parallel", "parallel", "arbitrary")),
    )(a, b)


def _in_proj(x, g1, w_in_t, b_in):
    tm = 1024

    def body(x_ref, g_ref, w_ref, b_ref, qkv_ref, uy_ref, gg_ref, h_ref, h_scr):
        j = pl.program_id(1)

        @pl.when(j == 0)
        def _():
            xv = x_ref[...]
            r = lax.rsqrt(jnp.mean(xv * xv, axis=-1, keepdims=True) + EPS)
            h = ((xv * r) * g_ref[...]).astype(BF16)
            h_scr[...] = h
            h_ref[...] = h

        z = _dot_nt(h_scr[...], w_ref[...]) + b_ref[...]

        @pl.when(j < 3)
        def _():
            qkv_ref[...] = z.astype(BF16)

        @pl.when((j >= 3) & (j < 7))
        def _():
            uy_ref[...] = z

        @pl.when(j >= 7)
        def _():
            gg_ref[...] = z

    return pl.pallas_call(
        body, name="in_proj",
        out_shape=(jax.ShapeDtypeStruct((T, 3 * D_ATT), BF16),
                   jax.ShapeDtypeStruct((T, 2 * D_REC), F32),
                   jax.ShapeDtypeStruct((T, 2 * D), F32),
                   jax.ShapeDtypeStruct((T, D), BF16)),
        grid=(T // tm, N_DZ_TILES),
        in_specs=[pl.BlockSpec((tm, D), lambda i, j: (i, 0)),
                  pl.BlockSpec((1, D), lambda i, j: (0, 0)),
                  pl.BlockSpec((TILE, D), lambda i, j: (j, 0)),
                  pl.BlockSpec((1, TILE), lambda i, j: (0, j))],
        out_specs=(pl.BlockSpec((tm, TILE), lambda i, j: (i, jnp.minimum(j, 2))),
                   pl.BlockSpec((tm, TILE), lambda i, j: (i, jnp.clip(j - 3, 0, 3))),
                   pl.BlockSpec((tm, TILE), lambda i, j: (i, jnp.clip(j - 7, 0, 3))),
                   pl.BlockSpec((tm, D), lambda i, j: (i, 0))),
        scratch_shapes=[pltpu.VMEM((tm, D), BF16)],
        compiler_params=_params(dimension_semantics=("parallel", "arbitrary")),
    )(x, g1, w_in_t, b_in)


def _dz_specs(rows, tile_of, row_of):
    def spec(off, n, per_plane):
        def index(*ids):
            t = jnp.clip(tile_of(*ids) - off, 0, n - 1)
            return (t // per_plane, row_of(*ids), t % per_plane)
        return pl.BlockSpec((1, rows, TILE), index)
    return [spec(off, n, per) for off, n, per in DZ_ARRAYS]


def _dh_norm1_bwd(dz, w_in_t, x, g1, dx1):
    tm = 1024

    def body(*refs):
        seg_refs = refs[:3]
        w_ref, x_ref, g_ref, dx1_ref, gx_ref, dg_ref, acc = refs[3:]
        i, kk = pl.program_id(0), pl.program_id(1)

        @pl.when(kk == 0)
        def _():
            acc[...] = jnp.zeros_like(acc)

        for s, (off, n, _) in enumerate(DZ_ARRAYS):
            @pl.when((kk >= off) & (kk < off + n))
            def _(s=s):
                acc[...] += _dot(seg_refs[s][0], w_ref[...])

        @pl.when((i == 0) & (kk == 0))
        def _():
            dg_ref[...] = jnp.zeros_like(dg_ref)

        @pl.when(kk == N_DZ_TILES - 1)
        def _():
            xv = x_ref[...]
            r = lax.rsqrt(jnp.mean(xv * xv, axis=-1, keepdims=True) + EPS)
            xh = xv * r
            dh = acc[...]
            dg_ref[...] += jnp.sum(dh * xh, axis=0, keepdims=True)
            gx_ref[...] = dx1_ref[...] + _rms_bwd(dh, xh, r, g_ref[...])

    tok = pl.BlockSpec((tm, D), lambda i, j: (i, 0))
    vec = pl.BlockSpec((1, D), lambda i, j: (0, 0))
    return pl.pallas_call(
        body, name="dh_norm1_bwd",
        out_shape=(jax.ShapeDtypeStruct((T, D), F32), jax.ShapeDtypeStruct((1, D), F32)),
        grid=(T // tm, N_DZ_TILES),
        in_specs=_dz_specs(tm, lambda i, j: j, lambda i, j: i)
        + [pl.BlockSpec((TILE, D), lambda i, j: (j, 0)), tok, vec, tok],
        out_specs=(tok, vec),
        scratch_shapes=[pltpu.VMEM((tm, D), F32)],
        compiler_params=_params(dimension_semantics=("arbitrary", "arbitrary")),
    )(*dz, w_in_t, x, g1, dx1)


def _grad_w_in(dz, h):
    def body(*refs):
        seg_refs = refs[:3]
        h_ref, gw_ref, gb_ref = refs[3:]
        j = pl.program_id(0)

        for s, (off, n, _) in enumerate(DZ_ARRAYS):
            @pl.when((j >= off) & (j < off + n))
            def _(s=s):
                a = seg_refs[s][0]
                gw_ref[...] = _dot_tn(a, h_ref[...]).astype(BF16)
                gb_ref[...] = jnp.sum(a.astype(F32), axis=0, keepdims=True)

    return pl.pallas_call(
        body, name="grad_w_in",
        out_shape=(jax.ShapeDtypeStruct((D_IN, D), BF16), jax.ShapeDtypeStruct((1, D_IN), F32)),
        grid=(N_DZ_TILES,),
        in_specs=_dz_specs(T, lambda j: j, lambda j: 0) + [pl.BlockSpec((T, D), lambda j: (0, 0))],
        out_specs=(pl.BlockSpec((TILE, D), lambda j: (j, 0)), pl.BlockSpec((1, TILE), lambda j: (0, j))),
        compiler_params=_params(dimension_semantics=("parallel",)),
    )(*dz, h)


def _rpb_rows(rpb):
    padded = jnp.pad(rpb, ((0, 0), (0, 0), (0, GRID_W - N_RPB_C)))
    rows = [padded[:, WIN_H - 1 - oi: 2 * WIN_H - 1 - oi].reshape(N_HEADS // 2, 2, KWIN)
            for oi in range(WIN_H)]
    return jnp.stack(rows, axis=0)


SKEW = KWIN - (WIN_W - 1)


MASKED = -1e30


def _bias_tiles(rows_ref, valid, bias_s):
    for oi in range(WIN_H):
        for hh in range(2):
            row = jnp.broadcast_to(rows_ref[oi, 0, hh:hh + 1, :], (GRID_W, KWIN))
            tile = pltpu.roll(row, SKEW, 1, stride=1, stride_axis=0)
            bias_s[oi, hh * GRID_W:(hh + 1) * GRID_W, :] = jnp.where(valid[:GRID_W], tile, MASKED)


def _bias_tile_grads(gb_s, flip, out_ref):
    for oi in range(WIN_H):
        for hh in range(2):
            g = _dot_exact(flip, gb_s[oi, hh * GRID_W:(hh + 1) * GRID_W, :])
            back = pltpu.roll(g, KWIN - (GRID_W - WIN_W), 1, stride=1, stride_axis=0)
            out_ref[0, oi, hh:hh + 1, :] = jnp.sum(back, axis=0, keepdims=True)


def _rpb_fold(row_grads):
    g = row_grads.transpose(1, 0, 2, 3).reshape(WIN_H, N_HEADS, WIN_H, GRID_W)
    g = g.transpose(0, 2, 1, 3)

    def body(g_ref, o_ref):
        for dr in range(N_RPB_R):
            terms = [g_ref[oi, i] for oi in range(WIN_H) for i in range(WIN_H) if i - oi + WIN_H - 1 == dr]
            acc = terms[0]
            for term in terms[1:]:
                acc = acc + term
            o_ref[dr] = acc

    out = pl.pallas_call(
        body, name="rpb_fold",
        out_shape=jax.ShapeDtypeStruct((N_RPB_R, N_HEADS, GRID_W), F32),
    )(g)
    return out.transpose(1, 0, 2)[:, :, :N_RPB_C]


def _att_scores(q_ref, k_ref, bias_ref, hmask, r):
    rs = jnp.clip(r - WIN_H // 2, 0, ROWS - WIN_H)
    oi = r - rs
    q0 = pl.multiple_of(r * GRID_W, GRID_W)
    k0 = pl.multiple_of(rs * GRID_W, GRID_W)
    q_r = q_ref[pl.ds(q0, GRID_W), :] * (DH ** -0.5)
    q2 = jnp.where(hmask, jnp.concatenate([q_r, q_r], axis=0), jnp.zeros((), BF16))
    kw = k_ref[pl.ds(k0, KWIN), :]
    s = _dot_nt(q2, kw) + bias_ref[oi]
    e = jnp.exp(s - jnp.max(s, axis=-1, keepdims=True))
    return e, 1.0 / jnp.sum(e, axis=-1, keepdims=True), q2, kw, q0, k0, oi


def _att_fwd(qkv, bias_rows):
    valid_np, hmask_np = _att_tables()

    def body(q_ref, k_ref, v_ref, rows_ref, valid_ref, hmask_ref, o_ref, bias_s):
        valid = valid_ref[...] > 0.5
        hmask = hmask_ref[...] > 0.5
        first_head = lax.broadcasted_iota(jnp.int32, (GRID_W, 2 * DH), 1) < DH
        _bias_tiles(rows_ref, valid, bias_s)

        def row(r, carry):
            e, rl, _, _, q0, k0, _ = _att_scores(q_ref, k_ref, bias_s, hmask, r)
            o2 = _dot((e * rl).astype(BF16), v_ref[pl.ds(k0, KWIN), :])
            o_ref[pl.ds(q0, GRID_W), :] = jnp.where(first_head, o2[:GRID_W], o2[GRID_W:]).astype(BF16)
            return carry

        lax.fori_loop(0, ROWS, row, 0, unroll=4)

    col = lambda off: pl.BlockSpec((T, 2 * DH), lambda hp: (0, hp + off))
    return pl.pallas_call(
        body, name="att_fwd",
        out_shape=jax.ShapeDtypeStruct((T, D_ATT), BF16),
        grid=(N_HEADS // 2,),
        in_specs=[col(0), col(4), col(8),
                  pl.BlockSpec((WIN_H, 1, 2, KWIN), lambda hp: (0, hp, 0, 0)),
                  pl.BlockSpec((2 * GRID_W, KWIN), lambda hp: (0, 0)),
                  pl.BlockSpec((2 * DH, 2 * DH), lambda hp: (0, 0))],
        out_specs=pl.BlockSpec((T, 2 * DH), lambda hp: (0, hp)),
        scratch_shapes=[pltpu.VMEM((WIN_H, 2 * GRID_W, KWIN), F32)],
        compiler_params=_params(dimension_semantics=("parallel",)),
    )(qkv, qkv, qkv, bias_rows, jnp.asarray(valid_np), jnp.asarray(hmask_np))


def _att_bwd(qkv, bias_rows, datt, after):
    valid_np, hmask_np = _att_tables()

    def body(q_ref, k_ref, v_ref, do_ref, rows_ref, valid_ref, hmask_ref, flip_ref,
             dqkv_ref, grows_ref, dk_acc, dv_acc, bias_s, gb_s):
        valid = valid_ref[...] > 0.5
        hmask = hmask_ref[...] > 0.5
        first_head = lax.broadcasted_iota(jnp.int32, (GRID_W, 2 * DH), 1) < DH
        dk_acc[...] = jnp.zeros_like(dk_acc)
        dv_acc[...] = jnp.zeros_like(dv_acc)
        gb_s[...] = jnp.zeros_like(gb_s)
        _bias_tiles(rows_ref, valid, bias_s)

        def row(r, carry):
            e, rl, q2, kw, q0, k0, oi = _att_scores(q_ref, k_ref, bias_s, hmask, r)
            do_r = do_ref[pl.ds(q0, GRID_W), :]
            do2 = jnp.where(hmask, jnp.concatenate([do_r, do_r], axis=0), jnp.zeros((), BF16))
            vw = v_ref[pl.ds(k0, KWIN), :]
            p = e * rl
            dp = _dot_nt(do2, vw)
            ds = p * (dp - jnp.sum(dp * p, axis=-1, keepdims=True))
            p16 = p.astype(BF16)
            ds16 = ds.astype(BF16)
            dv_acc[pl.ds(k0, KWIN), :] += _dot_tn(p16, do2)
            dk_acc[pl.ds(k0, KWIN), :] += _dot_tn(ds16, q2)
            dq2 = _dot(ds16, kw) * (DH ** -0.5)
            dqkv_ref[0, pl.ds(q0, GRID_W), :] = jnp.where(first_head, dq2[:GRID_W], dq2[GRID_W:]).astype(BF16)
            gb_s[oi] += ds
            return carry

        lax.fori_loop(0, ROWS, row, 0, unroll=4)
        dqkv_ref[1] = dk_acc[...].astype(BF16)
        dqkv_ref[2] = dv_acc[...].astype(BF16)
        _bias_tile_grads(gb_s, flip_ref[...], grows_ref)

    col = lambda off: pl.BlockSpec((T, 2 * DH), lambda hp: (0, hp + off))
    tiles = pltpu.VMEM((WIN_H, 2 * GRID_W, KWIN), F32)
    return pl.pallas_call(
        body, name="att_bwd",
        out_shape=(jax.ShapeDtypeStruct((3, T, D_ATT), BF16),
                   jax.ShapeDtypeStruct((N_HEADS // 2, WIN_H, 2, KWIN), F32)),
        grid=(N_HEADS // 2,),
        in_specs=[col(0), col(4), col(8), col(0),
                  pl.BlockSpec((WIN_H, 1, 2, KWIN), lambda hp: (0, hp, 0, 0)),
                  pl.BlockSpec((2 * GRID_W, KWIN), lambda hp: (0, 0)),
                  pl.BlockSpec((2 * DH, 2 * DH), lambda hp: (0, 0)),
                  pl.BlockSpec((GRID_W, GRID_W), lambda hp: (0, 0))],
        out_specs=(pl.BlockSpec((3, T, 2 * DH), lambda hp: (0, 0, hp)),
                   pl.BlockSpec((1, WIN_H, 2, KWIN), lambda hp: (hp, 0, 0, 0))),
        scratch_shapes=[pltpu.VMEM((T, 2 * DH), F32), pltpu.VMEM((T, 2 * DH), F32), tiles, tiles],
        compiler_params=_params(dimension_semantics=("parallel",)),
    )(qkv, qkv, qkv, datt, bias_rows, jnp.asarray(valid_np) + after, jnp.asarray(hmask_np),
      jnp.asarray(np.eye(GRID_W, dtype=np.float32)[::-1].copy()))


def _conv_taps(up):
    return (_shift_rows(up, 2), _shift_rows(up, 1), up, _shift_rows(up, -1))


def _pair_block_diag(w_pair, dup, same_half):
    return jnp.where(same_half, _dot(w_pair.astype(BF16), dup), 0.0).astype(BF16)


def _gates(u, u16, wa, ba, wi, bi, lam):
    r = _sigmoid(_dot(u16, wa) + ba)
    ig = _sigmoid(_dot(u16, wi) + bi)
    sp = _softplus(-lam)
    log_a = (-LRU_C) * r * sp
    a = jnp.exp(log_a)
    mult2 = jnp.maximum(_one_minus_square(log_a, a), 0.0)
    return r, ig, sp, a, jnp.sqrt(mult2), mult2


SCAN_BLOCKS = 2


def _scans(jobs):
    c = jobs[0][0].shape[1]
    nblk = T // 8
    rows = lax.broadcasted_iota(jnp.int32, (8, c), 0)

    def block(a, b, reverse):
        for s in (1, 2, 4):
            if reverse:
                keep = rows < 8 - s
                a_s = jnp.where(keep, pltpu.roll(a, 8 - s, 0), 1.0)
                b_s = jnp.where(keep, pltpu.roll(b, 8 - s, 0), 0.0)
            else:
                keep = rows >= s
                a_s = jnp.where(keep, pltpu.roll(a, s, 0), 1.0)
                b_s = jnp.where(keep, pltpu.roll(b, s, 0), 0.0)
            b = a * b_s + b
            a = a * a_s
        return a, b

    def step(i, carry):
        out = []
        for (a_ref, b_ref, h_ref, reverse), h_prev in zip(jobs, carry):
            for u in range(SCAN_BLOCKS):
                blk = i * SCAN_BLOCKS + u
                if reverse:
                    blk = nblk - 1 - blk
                t0 = pl.multiple_of(blk * 8, 8)
                a, b = block(a_ref[pl.ds(t0, 8), :], b_ref[pl.ds(t0, 8), :], reverse)
                h = a * h_prev + b
                h_ref[pl.ds(t0, 8), :] = h
                h_prev = jnp.broadcast_to(h[0:1] if reverse else h[7:8], (8, c))
            out.append(h_prev)
        return tuple(out)

    lax.fori_loop(0, nblk // SCAN_BLOCKS, step, tuple(jnp.zeros((8, c), F32) for _ in jobs))


def _rec_specs():
    tok = lambda off: pl.BlockSpec((T, CG), lambda g: (0, g + off))
    per_ch = lambda rows: pl.BlockSpec((rows, CG), lambda g: (0, g))
    wspec = pl.BlockSpec((2, 1, CG, REC_BLOCK), lambda g: (0, g, 0, 0))
    const = lambda shape: pl.BlockSpec(shape, lambda g: (0, 0))
    return tok, per_ch, wspec, const


def _rec_fwd(uy, conv_w, conv_b, w_a, b_a, w_i, b_i, lam):
    tok, per_ch, wspec, const = _rec_specs()

    def body(up_ref, yb_ref, cw_ref, cb_ref, wa_ref, ba_ref, wi_ref, bi_ref, lam_ref, dup_ref, half_ref,
             hf_ref, hb_ref, yrec_ref, a_f, bx_f, a_b, bx_b):
        dup = dup_ref[...]
        same_half = half_ref[...] > 0.5
        taps = _conv_taps(up_ref[...])
        u = cb_ref[...]
        for j in range(4):
            u = u + taps[j] * cw_ref[j:j + 1, :]
        u16 = u.astype(BF16)
        for d, (a_s, bx_s) in enumerate(((a_f, bx_f), (a_b, bx_b))):
            wa = _pair_block_diag(wa_ref[d, 0], dup, same_half)
            wi = _pair_block_diag(wi_ref[d, 0], dup, same_half)
            _, ig, _, a, mult, _ = _gates(u, u16, wa, ba_ref[d:d + 1, :], wi, bi_ref[d:d + 1, :],
                                       lam_ref[d:d + 1, :])
            a_s[...] = a
            bx_s[...] = mult * (ig * u)
        _scans([(a_f, bx_f, hf_ref, False), (a_b, bx_b, hb_ref, True)])
        gelu, _ = _gelu_and_grad(yb_ref[...])
        yrec_ref[...] = ((hf_ref[...] + hb_ref[...]) * gelu).astype(BF16)

    return pl.pallas_call(
        body, name="rec_fwd",
        out_shape=(jax.ShapeDtypeStruct((T, D_REC), F32), jax.ShapeDtypeStruct((T, D_REC), F32),
                   jax.ShapeDtypeStruct((T, D_REC), BF16)),
        grid=(N_CG,),
        in_specs=[tok(0), tok(N_CG), per_ch(4), per_ch(1), wspec, per_ch(2), wspec, per_ch(2), per_ch(2),
                  const((REC_BLOCK, CG)), const((CG, CG))],
        out_specs=(tok(0), tok(0), tok(0)),
        scratch_shapes=[pltpu.VMEM((T, CG), F32)] * 4,
        compiler_params=_params(dimension_semantics=("parallel",)),
    )(uy, uy, conv_w, conv_b, w_a, b_a, w_i, b_i, lam,
      jnp.asarray(_dup_table(), BF16), jnp.asarray(_pair_mask()))


def _rec_bwd(uy, hf, hb, dyrec, conv_w, conv_b, w_a, b_a, w_i, b_i, lam):
    tok, per_ch, wspec, const = _rec_specs()

    def body(up_ref, yb_ref, hf_ref, hb_ref, dy_ref, cw_ref, cb_ref, wa_ref, ba_ref, wi_ref, bi_ref,
             lam_ref, dup_ref, dupt_ref, half_ref,
             duy_ref, dcw_ref, dcb_ref, dwa_ref, dba_ref, dwi_ref, dbi_ref, dlam_ref,
             a_s0, a_s1, dh_s, g_s0, g_s1):
        dup = dup_ref[...]
        dup_t = dupt_ref[...]
        same_half = half_ref[...] > 0.5
        taps = _conv_taps(up_ref[...])
        u = cb_ref[...]
        for j in range(4):
            u = u + taps[j] * cw_ref[j:j + 1, :]
        u16 = u.astype(BF16)
        gelu, dgelu = _gelu_and_grad(yb_ref[...])
        dy = dy_ref[...]
        duy_ref[1] = (dy * (hf_ref[...] + hb_ref[...]) * dgelu).astype(BF16)
        dh_s[...] = dy * gelu
        gate_values = []
        for d, a_s in enumerate((a_s0, a_s1)):
            wa = _pair_block_diag(wa_ref[d, 0], dup, same_half)
            wi = _pair_block_diag(wi_ref[d, 0], dup, same_half)
            lam_d = lam_ref[d:d + 1, :]
            r, ig, sp, a, mult, mult2 = _gates(u, u16, wa, ba_ref[d:d + 1, :], wi, bi_ref[d:d + 1, :], lam_d)
            a_s[...] = _shift_rows(a, 1 if d == 1 else -1)
            gate_values.append((wa, wi, lam_d, r, ig, sp, a, mult, mult2))
        _scans([(a_s0, dh_s, g_s0, True), (a_s1, dh_s, g_s1, False)])
        du = jnp.zeros((T, CG), F32)
        for d, g_s in enumerate((g_s0, g_s1)):
            reverse = d == 1
            wa, wi, lam_d, r, ig, sp, a, mult, mult2 = gate_values[d]
            g = g_s[...]
            h_prev = _shift_rows(hb_ref[...], -1) if reverse else _shift_rows(hf_ref[...], 1)
            da = g * h_prev
            dmult = g * (ig * u)
            dig = g * mult * u
            du = du + g * mult * ig
            dmult_dlog = jnp.where(mult2 > 0.0, -(a * a) * lax.rsqrt(mult2), 0.0)
            dlog_a = da * a + dmult * dmult_dlog
            dr = dlog_a * ((-LRU_C) * sp)
            dsp = jnp.sum(dlog_a * ((-LRU_C) * r), axis=0, keepdims=True)
            dlam_ref[d:d + 1, :] = dsp * (-_sigmoid(-lam_d))
            dga = dr * r * (1.0 - r)
            dgi = dig * ig * (1.0 - ig)
            dga16 = dga.astype(BF16)
            dgi16 = dgi.astype(BF16)
            du = du + _dot_nt(dga16, wa) + _dot_nt(dgi16, wi)
            dwa_ref[d, 0] = _dot_exact(jnp.where(same_half, _dot_tn(u16, dga16), 0.0), dup_t)
            dwi_ref[d, 0] = _dot_exact(jnp.where(same_half, _dot_tn(u16, dgi16), 0.0), dup_t)
            dba_ref[d:d + 1, :] = jnp.sum(dga, axis=0, keepdims=True)
            dbi_ref[d:d + 1, :] = jnp.sum(dgi, axis=0, keepdims=True)
        dcb_ref[...] = jnp.sum(du, axis=0, keepdims=True)
        for j in range(4):
            dcw_ref[j:j + 1, :] = jnp.sum(du * taps[j], axis=0, keepdims=True)
        dup_in = (_shift_rows(du, -2) * cw_ref[0:1, :] + _shift_rows(du, -1) * cw_ref[1:2, :]
                  + du * cw_ref[2:3, :] + _shift_rows(du, 1) * cw_ref[3:4, :])
        duy_ref[0] = dup_in.astype(BF16)

    wshape = jax.ShapeDtypeStruct((2, N_CG, CG, REC_BLOCK), F32)
    vec = lambda rows: jax.ShapeDtypeStruct((rows, D_REC), F32)
    dup_np = _dup_table()
    return pl.pallas_call(
        body, name="rec_bwd",
        out_shape=(jax.ShapeDtypeStruct((2, T, D_REC), BF16),
                   vec(4), vec(1), wshape, vec(2), wshape, vec(2), vec(2)),
        grid=(N_CG,),
        in_specs=[tok(0), tok(N_CG), tok(0), tok(0), tok(0),
                  per_ch(4), per_ch(1), wspec, per_ch(2), wspec, per_ch(2), per_ch(2),
                  const((REC_BLOCK, CG)), const((CG, REC_BLOCK)), const((CG, CG))],
        out_specs=(pl.BlockSpec((2, T, CG), lambda g: (0, 0, g)),
                   per_ch(4), per_ch(1), wspec, per_ch(2), wspec, per_ch(2), per_ch(2)),
        scratch_shapes=[pltpu.VMEM((T, CG), F32)] * 5,
        compiler_params=_params(dimension_semantics=("parallel",)),
    )(uy, uy, hf, hb, dyrec, conv_w, conv_b, w_a, b_a, w_i, b_i, lam,
      jnp.asarray(dup_np, BF16), jnp.asarray(dup_np.T.copy()), jnp.asarray(_pair_mask()))


TM_MIX = 256


def _mix_specs():
    tok = lambda width, blk=0: pl.BlockSpec((TM_MIX, width), lambda i: (i, blk))
    full = lambda shape: pl.BlockSpec(shape, lambda i: (0, 0))
    return tok, full


def _mix_fwd(x, att, yrec, gg, w_att_o_t, w_rec_o, w_out):
    tok, full = _mix_specs()

    def body(x_ref, att_ref, yr_ref, ga_ref, gr_ref, wao_ref, wro_ref, wo_ref, x1_ref, mixed_ref):
        y_att = _dot_nt(att_ref[...], wao_ref[...])
        y_rec = _dot(yr_ref[...], wro_ref[...])
        mixed = (_sigmoid(ga_ref[...]) * y_att + _sigmoid(gr_ref[...]) * y_rec).astype(BF16)
        mixed_ref[...] = mixed
        x1_ref[...] = x_ref[...] + _dot(mixed, wo_ref[...])

    return pl.pallas_call(
        body, name="mix_fwd",
        out_shape=(jax.ShapeDtypeStruct((T, D), F32), jax.ShapeDtypeStruct((T, D), BF16)),
        grid=(T // TM_MIX,),
        in_specs=[tok(D), tok(D_ATT), tok(D_REC), tok(D, 0), tok(D, 1),
                  full((D, D_ATT)), full((D_REC, D)), full((D, D))],
        out_specs=(tok(D), tok(D)),
        compiler_params=_params(dimension_semantics=("parallel",)),
    )(x, att, yrec, gg, gg, w_att_o_t, w_rec_o, w_out)


def _mix_bwd(dx1, att, yrec, gg, w_att_o_t, w_rec_o, w_out):
    tok, full = _mix_specs()

    def body(dx_ref, att_ref, yr_ref, ga_ref, gr_ref, wao_ref, wro_ref, wo_ref,
             dgg_ref, dya_ref, dyr_ref, datt_ref, dyrp_ref):
        dmixed = _dot_nt(dx_ref[...].astype(BF16), wo_ref[...])
        y_att = _dot_nt(att_ref[...], wao_ref[...])
        y_rec = _dot(yr_ref[...], wro_ref[...])
        sa = _sigmoid(ga_ref[...])
        sr = _sigmoid(gr_ref[...])
        dgg_ref[0] = (dmixed * y_att * sa * (1.0 - sa)).astype(BF16)
        dgg_ref[1] = (dmixed * y_rec * sr * (1.0 - sr)).astype(BF16)
        dya = (dmixed * sa).astype(BF16)
        dyr = (dmixed * sr).astype(BF16)
        dya_ref[...] = dya
        dyr_ref[...] = dyr
        datt_ref[...] = _dot(dya, wao_ref[...]).astype(BF16)
        dyrp_ref[...] = _dot_nt(dyr, wro_ref[...])

    return pl.pallas_call(
        body, name="mix_bwd",
        out_shape=(jax.ShapeDtypeStruct((2, T, D), BF16),
                   jax.ShapeDtypeStruct((T, D), BF16), jax.ShapeDtypeStruct((T, D), BF16),
                   jax.ShapeDtypeStruct((T, D_ATT), BF16), jax.ShapeDtypeStruct((T, D_REC), F32)),
        grid=(T // TM_MIX,),
        in_specs=[tok(D), tok(D_ATT), tok(D_REC), tok(D, 0), tok(D, 1),
                  full((D, D_ATT)), full((D_REC, D)), full((D, D))],
        out_specs=(pl.BlockSpec((2, TM_MIX, D), lambda i: (0, i, 0)),
                   tok(D), tok(D), tok(D_ATT), tok(D_REC)),
        compiler_params=_params(dimension_semantics=("parallel",)),
    )(dx1, att, yrec, gg, gg, w_att_o_t, w_rec_o, w_out)


TM_FFN = 256
FF_CHUNK = 1024


def _ffn_loss(x1, target, g2, gf, w_ff1_t, w_ff2):
    n_chunks = D_FF // FF_CHUNK

    def body(x1_ref, tg_ref, g2_ref, gf_ref, w1_hbm, w2_hbm,
             loss_ref, dx1_ref, h2_ref, act_ref, dpre_ref, dx2_ref, dg2_ref, dgf_ref,
             w1, w2, relu_s):
        i = pl.program_id(0)

        @pl.when(i == 0)
        def _():
            pltpu.sync_copy(w1_hbm, w1)
            pltpu.sync_copy(w2_hbm, w2)
            loss_ref[...] = jnp.zeros_like(loss_ref)
            dg2_ref[...] = jnp.zeros_like(dg2_ref)
            dgf_ref[...] = jnp.zeros_like(dgf_ref)

        x1v = x1_ref[...]
        r2 = lax.rsqrt(jnp.mean(x1v * x1v, axis=-1, keepdims=True) + EPS)
        xh2 = x1v * r2
        h2 = (xh2 * g2_ref[...]).astype(BF16)
        h2_ref[...] = h2
        x2 = x1v
        for c in range(n_chunks):
            ff = slice(c * FF_CHUNK, (c + 1) * FF_CHUNK)
            rl = jnp.maximum(_dot_nt(h2, w1[ff, :]), 0.0)
            relu_s[:, ff] = rl
            act = (rl * rl).astype(BF16)
            act_ref[:, ff] = act
            x2 = x2 + _dot(act, w2[ff, :])
        r3 = lax.rsqrt(jnp.mean(x2 * x2, axis=-1, keepdims=True) + EPS)
        xh3 = x2 * r3
        err = xh3 * gf_ref[...] - tg_ref[...]
        loss_ref[...] += 0.5 * jnp.sum(jnp.mean(err * err, axis=-1, keepdims=True))
        dy = err * (1.0 / D)
        dgf_ref[...] += jnp.sum(dy * xh3, axis=0, keepdims=True)
        dx2 = _rms_bwd(dy, xh3, r3, gf_ref[...])
        dx2_16 = dx2.astype(BF16)
        dx2_ref[...] = dx2_16
        dh2 = jnp.zeros((TM_FFN, D), F32)
        for c in range(n_chunks):
            ff = slice(c * FF_CHUNK, (c + 1) * FF_CHUNK)
            dpre = (_dot_nt(dx2_16, w2[ff, :]) * (2.0 * relu_s[:, ff])).astype(BF16)
            dpre_ref[:, ff] = dpre
            dh2 = dh2 + _dot(dpre, w1[ff, :])
        dg2_ref[...] += jnp.sum(dh2 * xh2, axis=0, keepdims=True)
        dx1_ref[...] = dx2 + _rms_bwd(dh2, xh2, r2, g2_ref[...])

    tok = lambda width: pl.BlockSpec((TM_FFN, width), lambda i: (i, 0))
    vec = pl.BlockSpec((1, D), lambda i: (0, 0))
    hbm = pl.BlockSpec(memory_space=pl.ANY)
    return pl.pallas_call(
        body, name="ffn_loss",
        out_shape=(jax.ShapeDtypeStruct((8, 128), F32), jax.ShapeDtypeStruct((T, D), F32),
                   jax.ShapeDtypeStruct((T, D), BF16), jax.ShapeDtypeStruct((T, D_FF), BF16),
                   jax.ShapeDtypeStruct((T, D_FF), BF16), jax.ShapeDtypeStruct((T, D), BF16),
                   jax.ShapeDtypeStruct((1, D), F32), jax.ShapeDtypeStruct((1, D), F32)),
        grid=(T // TM_FFN,),
        in_specs=[tok(D), tok(D), vec, vec, hbm, hbm],
        out_specs=(pl.BlockSpec((8, 128), lambda i: (0, 0)), tok(D), tok(D), tok(D_FF), tok(D_FF), tok(D),
                   vec, vec),
        scratch_shapes=[pltpu.VMEM((D_FF, D), BF16), pltpu.VMEM((D_FF, D), BF16),
                        pltpu.VMEM((TM_FFN, D_FF), F32)],
        compiler_params=_params(dimension_semantics=("arbitrary",)),
    )(x1, target, g2, gf, w_ff1_t, w_ff2)


def _local_step(x, target, p, late_weights, reduce_early):
    bias = _rpb_rows(p["rpb"])
    pairs = lambda w: w.reshape(2, N_CG, CG, REC_BLOCK)
    w_a, w_i = pairs(p["w_rg_a"]), pairs(p["w_rg_i"])
    rec_params = (p["conv_w"], p["conv_b"], w_a, p["b_rg_a"], w_i, p["b_rg_i"], p["lru_lambda"])

    qkv, uy, gg, h = _in_proj(x, p["ln1_g"], p["w_in_t"], p["b_in"])
    att = _att_fwd(qkv, bias)
    hf, hb, yrec = _rec_fwd(uy, *rec_params)
    p = {**p, **late_weights(yrec)}
    x1, mixed = _mix_fwd(x, att, yrec, gg, p["w_att_o_t"], p["w_rec_o"], p["w_out"])
    loss8, dx1, h2, act, dpre, dx2, g_ln2, g_lnf = _ffn_loss(
        x1, target, p["ln2_g"], p["lnf_g"], p["w_ff1_t"], p["w_ff2"])

    dgg, dya, dyr, datt, dyrp = _mix_bwd(dx1, att, yrec, gg, p["w_att_o_t"], p["w_rec_o"], p["w_out"])
    duy, g_cw, g_cb, g_wa, g_ba, g_wi, g_bi, g_lam = _rec_bwd(uy, hf, hb, dyrp, *rec_params)
    blocks = lambda g: g.reshape(2, N_REC_BLOCKS, REC_BLOCK, REC_BLOCK)
    grads = {
        "w_att_o_t": _matmul(dya, att, "tn", BF16, "g_w_att_o"),
        "conv_w": g_cw, "conv_b": g_cb, "w_rg_a": blocks(g_wa), "b_rg_a": g_ba,
        "w_rg_i": blocks(g_wi), "b_rg_i": g_bi, "lru_lambda": g_lam,
        "w_rec_o": _matmul(yrec, dyr, "tn", BF16, "g_w_rec_o"),
        "w_out": _matmul(mixed, dx1, "tn", BF16, "g_w_out"),
        "ln2_g": g_ln2,
        "w_ff1_t": _matmul(dpre, h2, "tn", BF16, "g_w_ff1"),
        "w_ff2": _matmul(act, dx2, "tn", BF16, "g_w_ff2"),
        "lnf_g": g_lnf,
    }
    after = reduce_early(grads)
    dqkv, gbias = _att_bwd(qkv, bias, datt, after)
    dz = (dqkv, duy, dgg)
    grad_x, g_ln1 = _dh_norm1_bwd(dz, p["w_in_t"], x, p["ln1_g"], dx1)
    g_w_in_t, g_b_in = _grad_w_in(dz, h)
    grads.update(ln1_g=g_ln1, w_in_t=g_w_in_t, b_in=g_b_in, rpb=_rpb_fold(gbias))
    return loss8[0:1, 0:1], grad_x, grads


MESH_ID = pl.DeviceIdType.MESH
ANY = pl.BlockSpec(memory_space=pl.ANY)

CHAN_BLOCK_ROWS = 32
GATE_ROWS = 2 * 2 * N_REC_BLOCKS * REC_BLOCK * REC_BLOCK // (N_DEV * D)
SECTIONS = (("w_in_t", 704, D), ("w_rec_o", 128, D), ("w_out", 128, D), ("w_ff1_t", 512, D),
            ("w_ff2", 512, D), ("chan", CHAN_BLOCK_ROWS, D), ("w_att_o_t", 128, D_ATT),
            ("gates", GATE_ROWS, D))
N_SEC = len(SECTIONS)
N_CHAN_ROWS = 10
CHAN = (("conv_w", 4), ("b_rg_a", 2), ("b_rg_i", 2), ("lru_lambda", 2))


def _position():
    return lax.axis_index("x"), lax.axis_index("y"), lax.axis_index("c")


def _other_chips(x, y):
    return [(1 - x, y), (x, 1 - y), (1 - x, 1 - y)]


PASS_ON_ID, PAIR_EARLY_ID, PAIR_LATE_ID = 1, 2, 3


def _pair_handshake(x, y, c):
    barrier = pltpu.get_barrier_semaphore()
    pl.semaphore_signal(barrier, inc=1, device_id=(x, y, 1 - c), device_id_type=MESH_ID)
    pl.semaphore_wait(barrier, 1)


def _block_of(ref, dev, rows):
    return ref.at[pl.ds(pl.multiple_of(dev * rows, 16), rows)]


def _all_gather(shards, name):
    ns = len(shards)

    def body(*refs):
        x_refs, out_refs, done_ref = refs[:ns], refs[ns:2 * ns], refs[2 * ns]
        send_sems, recv_sems, local_sems = refs[2 * ns + 1:]
        done_ref[0, 0] = 0.0
        x, y, c = _position()
        me, sibling = (x, y, c), (x, y, 1 - c)
        x_nbr, y_nbr, diagonal = _other_chips(x, y)
        north = c == 1
        relay_from = (jnp.where(north, x_nbr[0], y_nbr[0]), jnp.where(north, x_nbr[1], y_nbr[1]))
        relay_to = (jnp.where(north, y_nbr[0], x_nbr[0]), jnp.where(north, y_nbr[1], x_nbr[1]))

        def rows(s, px, py, pc):
            return _block_of(out_refs[s], 4 * px + 2 * py + pc, shards[s].shape[0])

        def copy(k, s, block, to, from_shard=False):
            return pltpu.make_async_remote_copy(
                src_ref=x_refs[s] if from_shard else rows(s, *block), dst_ref=rows(s, *block),
                send_sem=send_sems.at[k * ns + s], recv_sem=recv_sems.at[k * ns + s],
                device_id=to, device_id_type=MESH_ID)

        sections = range(ns)
        mine = [pltpu.make_async_copy(x_refs[s], rows(s, *me), local_sems.at[s]) for s in sections]
        sent = [copy(k, s, me, to, True) for k, to in enumerate((sibling, (*x_nbr, c), (*y_nbr, c)))
                for s in sections]
        for cp in mine + sent:
            cp.start()
        for s in sections:
            copy(1, s, (*x_nbr, c), me).wait_recv()
            copy(2, s, (*y_nbr, c), me).wait_recv()
            sent += [copy(3, s, (*relay_from, c), (*relay_to, c)),
                     copy(4, s, (*x_nbr, c), sibling), copy(5, s, (*y_nbr, c), sibling)]
            for cp in sent[-3:]:
                cp.start()
        for s in sections:
            copy(3, s, (*diagonal, c), me).wait_recv()
            sent.append(copy(6, s, (*diagonal, c), sibling))
            sent[-1].start()
        for s in sections:
            copy(0, s, sibling, me).wait_recv()
            for k, chip in ((4, x_nbr), (5, y_nbr), (6, diagonal)):
                copy(k, s, (*chip, 1 - c), me).wait_recv()
        for cp in sent:
            cp.wait_send()
        for cp in mine:
            cp.wait()

    return pl.pallas_call(
        body, name=name,
        out_shape=tuple(jax.ShapeDtypeStruct((N_DEV * s.shape[0], s.shape[1]), s.dtype) for s in shards)
        + (jax.ShapeDtypeStruct((1, 1), F32),),
        in_specs=[ANY] * ns,
        out_specs=(ANY,) * ns + (pl.BlockSpec(memory_space=pltpu.SMEM),),
        scratch_shapes=[pltpu.SemaphoreType.DMA((7 * ns,)), pltpu.SemaphoreType.DMA((7 * ns,)),
                        pltpu.SemaphoreType.DMA((ns,))],
    )(*shards)


HBM = pl.BlockSpec(memory_space=pltpu.HBM)
SEM = pl.BlockSpec(memory_space=pltpu.SEMAPHORE)
EFFECT = pltpu.SideEffectType.DATAFLOW_SIDE_EFFECTING


def _in_hbm(a):
    return pltpu.with_memory_space_constraint(a, pltpu.HBM)


def _first_hop_copies(shards, x_refs, zones, send_sems, recv_sems):
    ns = len(shards)
    x, y, c = _position()
    targets = [(x, y, 1 - c)] + [(cx, cy, c) for cx, cy in _other_chips(x, y)]
    return [pltpu.make_async_remote_copy(
        src_ref=x_refs[s], dst_ref=_block_of(zones[s], 4 * x + 2 * y + c, shards[s].shape[0]),
        send_sem=send_sems.at[k * ns + s], recv_sem=recv_sems.at[k * ns + s],
        device_id=to, device_id_type=MESH_ID)
        for k, to in enumerate(targets) for s in range(ns)]


def _after_all(arrays, name):
    def body(*refs):
        refs[-1][...] = jnp.zeros_like(refs[-1])

    return pl.pallas_call(
        body, name=name,
        out_shape=jax.ShapeDtypeStruct((8, LANES), F32),
        in_specs=[pl.BlockSpec(memory_space=pl.ANY)] * len(arrays),
        out_specs=pl.BlockSpec(memory_space=pltpu.VMEM),
    )(*arrays)


def _own_blocks_placed(shards, after):
    ns = len(shards)
    x, y, c = _position()
    me = jnp.reshape(4 * x + 2 * y + c, (1,)).astype(jnp.int32)
    shards = [*shards[:-1], shards[-1] + after.astype(shards[-1].dtype)]

    def body(me_ref, *refs):
        for s in range(ns):
            refs[ns + s][...] = refs[s][...]

    return pl.pallas_call(
        body, name="own_blocks_placed",
        out_shape=tuple(jax.ShapeDtypeStruct((N_DEV * s.shape[0], s.shape[1]), s.dtype) for s in shards),
        grid_spec=pltpu.PrefetchScalarGridSpec(
            num_scalar_prefetch=1, grid=(1,),
            in_specs=[pl.BlockSpec(s.shape, lambda i, me: (0, 0)) for s in shards],
            out_specs=tuple(pl.BlockSpec(s.shape, lambda i, me: (me[0], 0)) for s in shards)),
        compiler_params=_params(dimension_semantics=("arbitrary",)),
    )(me, *shards)


def _gather_start(shards, after, name):
    ns = len(shards)
    zones = _own_blocks_placed(shards, after)

    def body(*refs):
        for cp in _first_hop_copies(shards, refs[:ns], refs[ns:2 * ns], refs[2 * ns], refs[2 * ns + 1]):
            cp.start()
        refs[-1][...] = jnp.zeros_like(refs[-1])

    out = pl.pallas_call(
        body, name=name,
        out_shape=(pltpu.SemaphoreType.DMA((4 * ns,)), pltpu.SemaphoreType.DMA((4 * ns,)),
                   *[pltpu.HBM(a.shape, a.dtype) for a in (*shards, *zones)],
                   jax.ShapeDtypeStruct((8, LANES), F32)),
        in_specs=[HBM] * (2 * ns),
        out_specs=(SEM, SEM, *[HBM] * (2 * ns), pl.BlockSpec(memory_space=pltpu.VMEM)),
        input_output_aliases={i: 2 + i for i in range(2 * ns)},
        compiler_params=pltpu.CompilerParams(has_side_effects=EFFECT),
    )(*[_in_hbm(a) for a in shards], *[_in_hbm(a) for a in zones])
    return out[0], out[1], out[2:2 + ns], out[2 + ns:2 + 2 * ns], out[-1]


def _gather_wait(send_sems, recv_sems, shards, zones, after, name):
    ns = len(shards)

    def body(*refs):
        for cp in _first_hop_copies(shards, refs[:ns], refs[ns:2 * ns], refs[2 * ns], refs[2 * ns + 1]):
            cp.wait_send()
            cp.wait_recv()

    out = pl.pallas_call(
        body, name=name,
        out_shape=tuple(pltpu.HBM(a.shape, a.dtype) for a in (*shards, *zones)),
        in_specs=[HBM] * (2 * ns) + [SEM, SEM, ANY],
        out_specs=(HBM,) * (2 * ns),
        input_output_aliases={i: i for i in range(2 * ns)},
        compiler_params=pltpu.CompilerParams(has_side_effects=EFFECT),
    )(*shards, *zones, send_sems, recv_sems, after)
    return out[ns:]


def _gather_pass_on(rows, zones, name):
    ns = len(zones)

    def body(*refs):
        in_refs, out_refs = refs[:ns], refs[ns:2 * ns]
        send_sems, recv_sems = refs[2 * ns:]
        x, y, c = _position()
        _pair_handshake(x, y, c)
        copies = [pltpu.make_async_remote_copy(
            src_ref=_block_of(in_refs[s], 4 * cx + 2 * cy + c, rows[s]),
            dst_ref=_block_of(out_refs[s], 4 * cx + 2 * cy + c, rows[s]),
            send_sem=send_sems.at[j * ns + s], recv_sem=recv_sems.at[j * ns + s],
            device_id=(x, y, 1 - c), device_id_type=MESH_ID)
            for j, (cx, cy) in enumerate(_other_chips(x, y)) for s in range(ns)]
        for cp in copies:
            cp.start()
        for cp in copies:
            cp.wait_recv()
        for cp in copies:
            cp.wait_send()

    return pl.pallas_call(
        body, name=name,
        out_shape=tuple(jax.ShapeDtypeStruct(z.shape, z.dtype) for z in zones),
        in_specs=[ANY] * ns, out_specs=(ANY,) * ns,
        input_output_aliases={i: i for i in range(ns)},
        scratch_shapes=[pltpu.SemaphoreType.DMA((3 * ns,)), pltpu.SemaphoreType.DMA((3 * ns,))],
        compiler_params=pltpu.CompilerParams(collective_id=PASS_ON_ID),
    )(*zones)


def _pair_exchange(sections, grads, barrier_id, name):
    ns = len(sections)

    def body(*refs):
        g_refs, land = refs[:ns], refs[ns:2 * ns]
        send_sems, recv_sems = refs[2 * ns:]
        x, y, c = _position()
        _pair_handshake(x, y, c)
        copies = [pltpu.make_async_remote_copy(
            src_ref=_block_of(g_refs[s], 2 * k + 1 - c, rows), dst_ref=land[s].at[k],
            send_sem=send_sems.at[k * ns + s], recv_sem=recv_sems.at[k * ns + s],
            device_id=(x, y, 1 - c), device_id_type=MESH_ID)
            for k in range(N_CHIPS) for s, (_, rows, _) in enumerate(sections)]
        for cp in copies:
            cp.start()
        for cp in copies:
            cp.wait_recv()
        for cp in copies:
            cp.wait_send()

    n = N_CHIPS * ns
    return pl.pallas_call(
        body, name=name,
        out_shape=tuple(jax.ShapeDtypeStruct((N_CHIPS, rows, cols), BF16) for _, rows, cols in sections),
        in_specs=[ANY] * ns, out_specs=(ANY,) * ns,
        scratch_shapes=[pltpu.SemaphoreType.DMA((n,)), pltpu.SemaphoreType.DMA((n,))],
        compiler_params=pltpu.CompilerParams(collective_id=barrier_id),
    )(*grads)


def _pair_add(sections, grads, got, core, name):
    ns = len(sections)

    def body(core_ref, *refs):
        g_refs, got_refs, p_refs = refs[:ns], refs[ns:2 * ns], refs[2 * ns:]
        for s in range(ns):
            p_refs[s][0] = (g_refs[s][...].astype(F32) + got_refs[s][0].astype(F32)).astype(BF16)

    slot = [pl.BlockSpec((1, rows, cols), lambda k, c: (k, 0, 0)) for _, rows, cols in sections]
    return pl.pallas_call(
        body, name=name,
        out_shape=tuple(jax.ShapeDtypeStruct((N_CHIPS, rows, cols), BF16) for _, rows, cols in sections),
        grid_spec=pltpu.PrefetchScalarGridSpec(
            num_scalar_prefetch=1, grid=(N_CHIPS,),
            in_specs=[pl.BlockSpec((rows, cols), lambda k, c: (2 * k + c[0], 0)) for _, rows, cols in sections]
            + slot,
            out_specs=tuple(slot)),
        compiler_params=_params(dimension_semantics=("parallel",)),
    )(core, *grads, *got)


def _chip_copies(sections, p_refs, land, send_sems, recv_sems):
    ns = len(sections)
    x, y, c = _position()
    return [pltpu.make_async_remote_copy(
        src_ref=p_refs[s].at[2 * cx + cy], dst_ref=land[s].at[j],
        send_sem=send_sems.at[j * ns + s], recv_sem=recv_sems.at[j * ns + s],
        device_id=(cx, cy, c), device_id_type=MESH_ID)
        for j, (cx, cy) in enumerate(_other_chips(x, y)) for s in range(ns)]


def _chip_exchange(sections, parts, name):
    ns = len(sections)

    def body(*refs):
        copies = _chip_copies(sections, refs[:ns], refs[ns:2 * ns], *refs[2 * ns:])
        for cp in copies:
            cp.start()
        for cp in copies:
            cp.wait_recv()
        for cp in copies:
            cp.wait_send()

    n = 3 * ns
    return pl.pallas_call(
        body, name=name,
        out_shape=tuple(jax.ShapeDtypeStruct((3, rows, cols), BF16) for _, rows, cols in sections),
        in_specs=[ANY] * ns, out_specs=(ANY,) * ns,
        scratch_shapes=[pltpu.SemaphoreType.DMA((n,)), pltpu.SemaphoreType.DMA((n,))],
    )(*parts)


def _chip_exchange_start(sections, parts, name):
    ns = len(sections)

    def body(*refs):
        p_refs, land = refs[:ns], refs[ns:2 * ns]
        send_sems, recv_sems = refs[2 * ns], refs[2 * ns + 1]
        token = refs[-1]
        for cp in _chip_copies(sections, p_refs, land, send_sems, recv_sems):
            cp.start()
        token[...] = jnp.zeros_like(token)

    zones = [lax.empty((3, rows, cols), BF16) for _, rows, cols in sections]
    out = pl.pallas_call(
        body, name=name,
        out_shape=(pltpu.SemaphoreType.DMA((3 * ns,)), pltpu.SemaphoreType.DMA((3 * ns,)),
                   *[pltpu.HBM(a.shape, a.dtype) for a in parts], *[pltpu.HBM(a.shape, a.dtype) for a in zones],
                   jax.ShapeDtypeStruct((8, LANES), F32)),
        in_specs=[HBM] * (2 * ns),
        out_specs=(SEM, SEM, *[HBM] * (2 * ns), pl.BlockSpec(memory_space=pltpu.VMEM)),
        input_output_aliases={i: 2 + i for i in range(2 * ns)},
        compiler_params=pltpu.CompilerParams(has_side_effects=EFFECT),
    )(*[_in_hbm(a) for a in parts], *[_in_hbm(a) for a in zones])
    return out[0], out[1], out[2:2 + ns], out[2 + ns:2 + 2 * ns], out[-1]


def _chip_exchange_wait(sections, send_sems, recv_sems, parts, zones, after, name):
    ns = len(sections)

    def body(*refs):
        p_refs, land = refs[:ns], refs[ns:2 * ns]
        for cp in _chip_copies(sections, p_refs, land, refs[2 * ns], refs[2 * ns + 1]):
            cp.wait_send()
            cp.wait_recv()

    out = pl.pallas_call(
        body, name=name,
        out_shape=tuple(pltpu.HBM(a.shape, a.dtype) for a in (*parts, *zones)),
        in_specs=[HBM] * (2 * ns) + [SEM, SEM, ANY],
        out_specs=(HBM,) * (2 * ns),
        input_output_aliases={i: i for i in range(2 * ns)},
        compiler_params=pltpu.CompilerParams(has_side_effects=EFFECT),
    )(*parts, *zones, send_sems, recv_sems, after)
    return out[:ns], out[ns:]


def _grad_finish(sections, parts, far, chip, name):
    ns = len(sections)

    def body(chip_ref, *refs):
        p_refs, b_refs, g_refs = refs[:ns], refs[ns:2 * ns], refs[2 * ns:]
        for s in range(ns):
            g = p_refs[s][0].astype(F32)
            for j in range(3):
                g = g + b_refs[s][j].astype(F32)
            g_refs[s][...] = g

    half = [(rows // 2, cols) for _, rows, cols in sections]
    return pl.pallas_call(
        body, name=name,
        out_shape=tuple(jax.ShapeDtypeStruct((rows, cols), F32) for _, rows, cols in sections),
        grid_spec=pltpu.PrefetchScalarGridSpec(
            num_scalar_prefetch=1, grid=(2,),
            in_specs=[pl.BlockSpec((1, r, c), lambda i, chip: (chip[0], i, 0)) for r, c in half]
            + [pl.BlockSpec((3, r, c), lambda i, chip: (0, i, 0)) for r, c in half],
            out_specs=tuple(pl.BlockSpec((r, c), lambda i, chip: (i, 0)) for r, c in half)),
        compiler_params=_params(dimension_semantics=("parallel",)),
    )(chip, *parts, *far)


def _sum_devices(parts, rows, name):
    cols = parts.shape[1]
    tr = rows // 2

    def body(*refs):
        s = refs[0][...].astype(F32)
        for d in range(1, N_DEV):
            s = s + refs[d][...].astype(F32)
        refs[N_DEV][...] = s

    return pl.pallas_call(
        body, name=name,
        out_shape=jax.ShapeDtypeStruct((rows, cols), F32),
        grid=(2,),
        in_specs=[pl.BlockSpec((tr, cols), lambda i, d=d: (2 * d + i, 0)) for d in range(N_DEV)],
        out_specs=pl.BlockSpec((tr, cols), lambda i: (i, 0)),
        compiler_params=_params(dimension_semantics=("parallel",)),
    )(*([parts] * N_DEV))


def _adamw_step(w_ref, g_ref, m_ref, v_ref, d_ref, nm_ref, nv_ref):
    c1 = 1.0 / (1.0 - ADAM_B1 ** ADAM_STEP)
    c2 = 1.0 / (1.0 - ADAM_B2 ** ADAM_STEP)
    gv = g_ref[...]
    nm = ADAM_B1 * m_ref[...] + (1.0 - ADAM_B1) * gv
    nv = ADAM_B2 * v_ref[...] + (1.0 - ADAM_B2) * (gv * gv)
    nm_ref[...] = nm
    nv_ref[...] = nv
    d_ref[...] = (-ADAM_LR) * ((nm * c1) / (jnp.sqrt(nv * c2) + ADAM_EPS) + ADAM_WD * w_ref[...])


def _adamw_small(params, name):
    n = len(params)

    def body(*refs):
        for k in range(n):
            _adamw_step(*refs[4 * k:4 * k + 4], *refs[4 * n + 3 * k:4 * n + 3 * k + 3])

    out = pl.pallas_call(
        body, name=name,
        out_shape=tuple(jax.ShapeDtypeStruct(p[0].shape, F32) for p in params for _ in range(3)),
    )(*[a for p in params for a in p])
    return [out[3 * k:3 * k + 3] for k in range(n)]


def _adamw(w, g, m, v, name):
    rows, cols = w.shape
    tr = rows
    while tr * cols * 4 > (1 << 20) and tr % 16 == 0:
        tr //= 2

    def body(*refs):
        _adamw_step(*refs)

    spec = pl.BlockSpec((tr, cols), lambda i: (i, 0))
    shape = jax.ShapeDtypeStruct((rows, cols), F32)
    return pl.pallas_call(
        body, name=name,
        out_shape=(shape, shape, shape),
        grid=(rows // tr,),
        in_specs=[spec] * 4, out_specs=(spec,) * 3,
        compiler_params=_params(dimension_semantics=("parallel",)),
    )(w, g, m, v)


NAMES = ("ln1_g", "w_in", "b_in", "rpb", "w_att_o", "conv_w", "conv_b", "w_rg_a", "b_rg_a", "w_rg_i",
         "b_rg_i", "lru_lambda", "w_rec_o", "w_out", "ln2_g", "w_ff1", "w_ff2", "lnf_g")
TRANSPOSED = {"w_in": "w_in_t", "w_att_o": "w_att_o_t", "w_ff1": "w_ff1_t"}
ROW_SHARDED = ("w_rec_o", "w_out", "w_ff2")
REPLICATED = (("ln1_g", (1, D)), ("b_in", (1, D_IN)), ("rpb", (N_HEADS * N_RPB_R, N_RPB_C)),
              ("conv_b", (1, D_REC)), ("w_rg_a", (2 * N_REC_BLOCKS * REC_BLOCK, REC_BLOCK)),
              ("w_rg_i", (2 * N_REC_BLOCKS * REC_BLOCK, REC_BLOCK)), ("ln2_g", (1, D)), ("lnf_g", (1, D)))
GATE_BLOCKS = ("w_rg_a", "w_rg_i")
SMALL_ROWS = 112


def _chan_bits(vectors):
    chan = jnp.concatenate(vectors, axis=0)
    bits = lax.bitcast_convert_type(chan, BF16).reshape(-1)
    return jnp.pad(bits, (0, CHAN_BLOCK_ROWS * D - bits.shape[0])).reshape(CHAN_BLOCK_ROWS, D)


def _chan_from_bits(gathered):
    bits = gathered.reshape(N_DEV, CHAN_BLOCK_ROWS * D)[:, :2 * N_CHAN_ROWS * LANES]
    chan = lax.bitcast_convert_type(bits.reshape(N_DEV, N_CHAN_ROWS, LANES, 2), F32)
    return chan.transpose(1, 0, 2).reshape(N_CHAN_ROWS, D)


def kernel(x, ln1_g, w_in, b_in, rpb, w_att_o, conv_w, conv_b, w_rg_a, b_rg_a, w_rg_i, b_rg_i, lru_lambda, w_rec_o, w_out, ln2_g, w_ff1, w_ff2, lnf_g, loss_target, m_ln1_g, m_w_in, m_b_in, m_rpb, m_w_att_o, m_conv_w, m_conv_b, m_w_rg_a, m_b_rg_a, m_w_rg_i, m_b_rg_i, m_lru_lambda, m_w_rec_o, m_w_out, m_ln2_g, m_w_ff1, m_w_ff2, m_lnf_g, v_ln1_g, v_w_in, v_b_in, v_rpb, v_w_att_o, v_conv_w, v_conv_b, v_w_rg_a, v_b_rg_a, v_w_rg_i, v_b_rg_i, v_lru_lambda, v_w_rec_o, v_w_out, v_ln2_g, v_w_ff1, v_w_ff2, v_lnf_g):
    w = dict(zip(NAMES, (ln1_g, w_in, b_in, rpb, w_att_o, conv_w, conv_b, w_rg_a, b_rg_a, w_rg_i,
                         b_rg_i, lru_lambda, w_rec_o, w_out, ln2_g, w_ff1, w_ff2, lnf_g)))
    m = dict(zip(NAMES, (m_ln1_g, m_w_in, m_b_in, m_rpb, m_w_att_o, m_conv_w, m_conv_b, m_w_rg_a,
                         m_b_rg_a, m_w_rg_i, m_b_rg_i, m_lru_lambda, m_w_rec_o, m_w_out, m_ln2_g,
                         m_w_ff1, m_w_ff2, m_lnf_g)))
    v = dict(zip(NAMES, (v_ln1_g, v_w_in, v_b_in, v_rpb, v_w_att_o, v_conv_w, v_conv_b, v_w_rg_a,
                         v_b_rg_a, v_w_rg_i, v_b_rg_i, v_lru_lambda, v_w_rec_o, v_w_out, v_ln2_g,
                         v_w_ff1, v_w_ff2, v_lnf_g)))
    xi, yi, ci = _position()

    shard = {t: w[n][0].T.astype(BF16) for n, t in TRANSPOSED.items()}
    shard.update({n: w[n][0].astype(BF16) for n in ROW_SHARDED})
    shard["chan"] = _chan_bits([w[n][0] for n, _ in CHAN])
    first, later = ("w_in_t", "chan"), ("w_rec_o", "w_out", "w_att_o_t", "w_ff1_t", "w_ff2")
    *gathered, done = _all_gather([shard[n] for n in first], "weight_all_gather")
    p = dict(zip(first, gathered))
    send_sems, recv_sems, sent, zones, token = _gather_start([shard[n] for n in later], done,
                                                             "weight_gather_start")

    def late_weights(after):
        landed = _gather_wait(send_sems, recv_sems, sent, zones, after, "weight_gather_wait")
        return dict(zip(later, _gather_pass_on([shard[n].shape[0] for n in later], landed,
                                               "weight_gather_pass_on")))

    chan = _chan_from_bits(p.pop("chan"))
    r0 = 0
    for n, rows in CHAN:
        p[n] = chan[r0:r0 + rows]
        r0 += rows
    p.update(ln1_g=w["ln1_g"], b_in=w["b_in"] + token[0, 0], rpb=w["rpb"][0], conv_b=w["conv_b"],
             w_rg_a=w["w_rg_a"][0], w_rg_i=w["w_rg_i"][0], ln2_g=w["ln2_g"],
             lnf_g=w["lnf_g"].reshape(1, D))

    core = jnp.reshape(ci, (1,)).astype(jnp.int32)
    chip = jnp.reshape(2 * xi + yi, (1,)).astype(jnp.int32)
    early_sections, late_sections = SECTIONS[1:], SECTIONS[:1]
    in_flight = {}

    def reduce_early(grads):
        chan_g = jnp.concatenate([grads[n] for n, _ in CHAN], axis=0)
        chan_g = chan_g.reshape(N_CHAN_ROWS, N_DEV, LANES).transpose(1, 0, 2).astype(BF16)
        chan_g = jnp.pad(chan_g.reshape(N_DEV, -1), ((0, 0), (0, CHAN_BLOCK_ROWS * D - N_CHAN_ROWS * LANES)))
        grads["chan"] = chan_g.reshape(N_DEV * CHAN_BLOCK_ROWS, D)
        grads["gates"] = jnp.concatenate([grads[n].reshape(-1, D) for n in GATE_BLOCKS], axis=0).astype(BF16)
        sect = [grads[n] for n, _, _ in early_sections]
        got = _pair_exchange(early_sections, sect, PAIR_EARLY_ID, "grad_pair_exchange_early")
        parts = _pair_add(early_sections, sect, got, core, "grad_pair_add_early")
        in_flight["early"] = _chip_exchange_start(early_sections, parts, "grad_chip_exchange_start")
        return in_flight["early"][-1][0, 0]

    loss_part, grad_x, grads = _local_step(x[0], loss_target[0], p, late_weights, reduce_early)
    sect = [grads[n] for n, _, _ in late_sections]
    got = _pair_exchange(late_sections, sect, PAIR_LATE_ID, "grad_pair_exchange_late")
    late_parts = _pair_add(late_sections, sect, got, core, "grad_pair_add_late")
    in_flight["late"] = _chip_exchange_start(late_sections, late_parts, "grad_chip_exchange_start_late")

    def finish(group, sections, after, name):
        send_sems, recv_sems, parts, zones, _ = in_flight[group]
        parts, far = _chip_exchange_wait(sections, send_sems, recv_sems, parts, zones, after,
                                         "grad_chip_exchange_wait_" + name)
        return dict(zip((n for n, _, _ in sections),
                        _grad_finish(sections, parts, far, chip, "grad_finish_" + name)))

    started_late = in_flight["late"][-1]
    summed = finish("early", early_sections, started_late, "early")

    flat = jnp.concatenate([grads[n].reshape(-1) for n, _ in REPLICATED if n not in GATE_BLOCKS]
                           + [loss_part.reshape(-1) + started_late[0, 0]])
    n_small = flat.shape[0]
    flat = jnp.pad(flat, (0, SMALL_ROWS * LANES - n_small)).reshape(SMALL_ROWS, LANES)
    small_parts, gate_sum, _ = _all_gather([flat, summed["gates"]], "small_grad_all_gather")
    small = _sum_devices(small_parts, SMALL_ROWS, "small_grad_sum").reshape(-1)
    loss = small[n_small - 1]

    g, delta, new_m, new_v = {}, {}, {}, {}

    def update(n, g2, shape2):
        d2, m2, v2 = _adamw(w[n].reshape(shape2), g2, m[n].reshape(shape2), v[n].reshape(shape2),
                            "adamw_" + n)
        g[n], delta[n], new_m[n], new_v[n] = (a.reshape(w[n].shape) for a in (g2, d2, m2, v2))

    small_params = []
    o = 0
    for n, shape2 in REPLICATED:
        if n in GATE_BLOCKS:
            k, rows = GATE_BLOCKS.index(n), gate_sum.shape[0] // len(GATE_BLOCKS)
            update(n, gate_sum[k * rows:(k + 1) * rows].reshape(shape2), shape2)
        else:
            size = shape2[0] * shape2[1]
            small_params.append((n, small[o:o + size].reshape(shape2), shape2))
            o += size
    chan_back = summed["chan"].reshape(-1)[:N_CHAN_ROWS * LANES].reshape(N_CHAN_ROWS, LANES)
    r0 = 0
    for n, rows in CHAN:
        small_params.append((n, chan_back[r0:r0 + rows], (rows, LANES)))
        r0 += rows
    results = _adamw_small([(w[n].reshape(s2), g2, m[n].reshape(s2), v[n].reshape(s2))
                            for n, g2, s2 in small_params], "adamw_vectors")
    for (n, g2, _), (d2, m2, v2) in zip(small_params, results):
        g[n], delta[n], new_m[n], new_v[n] = (a.reshape(w[n].shape) for a in (g2, d2, m2, v2))

    for n in ROW_SHARDED:
        update(n, summed[n], summed[n].shape)
    for n, t in TRANSPOSED.items():
        if t in summed:
            update(n, summed[t].T, summed[t].shape[::-1])
    summed = finish("late", late_sections, _after_all(list(delta.values()), "updates_done"), "late")
    update("w_in", summed["w_in_t"].T, summed["w_in_t"].shape[::-1])

    return (loss, grad_x[None], *[g[n] for n in NAMES], *[delta[n] for n in NAMES],
            *[new_m[n] for n in NAMES], *[new_v[n] for n in NAMES])
```

```python
import math

import numpy as np
import jax
import jax.numpy as jnp
from jax import lax
from jax.experimental import pallas as pl
from jax.experimental.pallas import tpu as pltpu

F32 = jnp.float32
BF16 = jnp.bfloat16

T = 2048
D = 1024
D_ATT = 512
D_REC = 1024
D_FF = 4096
D_IN = 5632
N_HEADS = 8
DH = 64
GRID_W = 64
ROWS = T // GRID_W
WIN_H = 8
WIN_W = 16
KWIN = WIN_H * GRID_W
N_RPB_R = 2 * WIN_H - 1
N_RPB_C = 2 * WIN_W - 1
N_REC_BLOCKS = 16
REC_BLOCK = 64
CG = 128
N_CG = D_REC // CG
LRU_C = 8.0
EPS = 1e-6
N_DEV = 8
N_CHIPS = 4
LANES = 128

ADAM_LR = 0.001
ADAM_B1 = 0.9
ADAM_B2 = 0.999
ADAM_EPS = 1e-08
ADAM_WD = 0.01
ADAM_STEP = 10

MESH_AXES = ("x", "y", "c")
VMEM_LIMIT = 56 * 1024 * 1024

TILE = 512
DZ_ARRAYS = ((0, 3, 1), (3, 4, 2), (7, 4, 2))
N_DZ_TILES = D_IN // TILE


def _params(**kw):
    return pltpu.CompilerParams(vmem_limit_bytes=VMEM_LIMIT, **kw)


def _att_tables():
    rq = np.arange(2 * GRID_W) % GRID_W
    kc = np.arange(KWIN) % GRID_W
    win_start = np.clip(rq - WIN_W // 2, 0, GRID_W - WIN_W)
    valid = (kc[None, :] >= win_start[:, None]) & (kc[None, :] < win_start[:, None] + WIN_W)
    return valid.astype(np.float32), _pair_mask()


def _pair_mask():
    half = np.arange(2 * DH) // DH
    return (half[:, None] == half[None, :]).astype(np.float32)


def _dup_table():
    return np.concatenate([np.eye(REC_BLOCK, dtype=np.float32)] * 2, axis=1)


def _sigmoid(x):
    return 0.5 * jnp.tanh(0.5 * x) + 0.5


def _softplus(x):
    return jnp.maximum(x, 0.0) + jnp.log(1.0 + jnp.exp(-jnp.abs(x)))


def _one_minus_square(log_a, a):
    x = 2.0 * log_a
    series = -x * (1.0 + x * (0.5 + x * (1.0 / 6.0)))
    return jnp.where(x > -0.02, series, 1.0 - a * a)


_GELU_C = math.sqrt(2.0 / math.pi)


def _gelu_and_grad(x):
    x2 = x * x
    inner = _GELU_C * (x + 0.044715 * x * x2)
    t = jnp.tanh(inner)
    g = 0.5 * x * (1.0 + t)
    dg = 0.5 * (1.0 + t) + 0.5 * x * (1.0 - t * t) * _GELU_C * (1.0 + 3.0 * 0.044715 * x2)
    return g, dg


def _dot(a, b):
    return jnp.dot(a, b, preferred_element_type=F32)


def _dot_nt(a, b):
    return lax.dot_general(a, b, (((1,), (1,)), ((), ())), preferred_element_type=F32)


def _dot_tn(a, b):
    return lax.dot_general(a, b, (((0,), (0,)), ((), ())), preferred_element_type=F32)


def _dot_exact(a, b):
    return jnp.dot(a, b, precision=lax.Precision.HIGHEST, preferred_element_type=F32)


def _shift_rows(x, s):
    n = x.shape[0]
    rows = lax.broadcasted_iota(jnp.int32, x.shape, 0)
    y = pltpu.roll(x, s % n, 0)
    if s > 0:
        return jnp.where(rows >= s, y, 0.0)
    return jnp.where(rows < n + s, y, 0.0)


def _rms_bwd(dh, xh, r, g):
    dxh = dh * g
    return r * (dxh - xh * jnp.mean(dxh * xh, axis=-1, keepdims=True))


def _matmul(a, b, mode, out_dtype, name, tm=512, tn=1024, tk=2048):
    if mode == "nn":
        (m, k), (k2, n) = a.shape, b.shape
    elif mode == "nt":
        (m, k), (n, k2) = a.shape, b.shape
    else:
        (k, m), (k2, n) = a.shape, b.shape
    assert k == k2
    tm, tn, tk = min(tm, m), min(tn, n), min(tk, k)
    assert m % tm == 0 and n % tn == 0 and k % tk == 0
    nk = k // tk
    dot = {"nn": _dot, "nt": _dot_nt, "tn": _dot_tn}[mode]

    def body(a_ref, b_ref, o_ref, acc):
        kk = pl.program_id(2)
        part = dot(a_ref[...].astype(BF16), b_ref[...].astype(BF16))
        if nk == 1:
            o_ref[...] = part.astype(out_dtype)
            return

        @pl.when(kk == 0)
        def _():
            acc[...] = part

        @pl.when(kk > 0)
        def _():
            acc[...] += part

        @pl.when(kk == nk - 1)
        def _():
            o_ref[...] = acc[...].astype(out_dtype)

    if mode == "tn":
        a_spec = pl.BlockSpec((tk, tm), lambda i, j, kk: (kk, i))
    else:
        a_spec = pl.BlockSpec((tm, tk), lambda i, j, kk: (i, kk))
    if mode == "nt":
        b_spec = pl.BlockSpec((tn, tk), lambda i, j, kk: (j, kk))
    else:
        b_spec = pl.BlockSpec((tk, tn), lambda i, j, kk: (kk, j))
    return pl.pallas_call(
        body, name=name,
        out_shape=jax.ShapeDtypeStruct((m, n), out_dtype),
        grid=(m // tm, n // tn, nk),
        in_specs=[a_spec, b_spec],
        out_specs=pl.BlockSpec((tm, tn), lambda i, j, kk: (i, j)),
        scratch_shapes=[pltpu.VMEM((tm, tn) if nk > 1 else (8, LANES), F32)],
        compiler_params=_params(dimension_semantics=("parallel", "parallel", "arbitrary")),
    )(a, b)


def _in_proj(x, g1, w_in_t, b_in):
    tm = 1024

    def body(x_ref, g_ref, w_ref, b_ref, qkv_ref, uy_ref, gg_ref, h_ref, h_scr):
        j = pl.program_id(1)

        @pl.when(j == 0)
        def _():
            xv = x_ref[...]
            r = lax.rsqrt(jnp.mean(xv * xv, axis=-1, keepdims=True) + EPS)
            h = ((xv * r) * g_ref[...]).astype(BF16)
            h_scr[...] = h
            h_ref[...] = h

        z = _dot_nt(h_scr[...], w_ref[...]) + b_ref[...]

        @pl.when(j < 3)
        def _():
            qkv_ref[...] = z.astype(BF16)

        @pl.when((j >= 3) & (j < 7))
        def _():
            uy_ref[...] = z

        @pl.when(j >= 7)
        def _():
            gg_ref[...] = z

    return pl.pallas_call(
        body, name="in_proj",
        out_shape=(jax.ShapeDtypeStruct((T, 3 * D_ATT), BF16),
                   jax.ShapeDtypeStruct((T, 2 * D_REC), F32),
                   jax.ShapeDtypeStruct((T, 2 * D), F32),
                   jax.ShapeDtypeStruct((T, D), BF16)),
        grid=(T // tm, N_DZ_TILES),
        in_specs=[pl.BlockSpec((tm, D), lambda i, j: (i, 0)),
                  pl.BlockSpec((1, D), lambda i, j: (0, 0)),
                  pl.BlockSpec((TILE, D), lambda i, j: (j, 0)),
                  pl.BlockSpec((1, TILE), lambda i, j: (0, j))],
        out_specs=(pl.BlockSpec((tm, TILE), lambda i, j: (i, jnp.minimum(j, 2))),
                   pl.BlockSpec((tm, TILE), lambda i, j: (i, jnp.clip(j - 3, 0, 3))),
                   pl.BlockSpec((tm, TILE), lambda i, j: (i, jnp.clip(j - 7, 0, 3))),
                   pl.BlockSpec((tm, D), lambda i, j: (i, 0))),
        scratch_shapes=[pltpu.VMEM((tm, D), BF16)],
        compiler_params=_params(dimension_semantics=("parallel", "arbitrary")),
    )(x, g1, w_in_t, b_in)


def _dz_specs(rows, tile_of, row_of):
    def spec(off, n, per_plane):
        def index(*ids):
            t = jnp.clip(tile_of(*ids) - off, 0, n - 1)
            return (t // per_plane, row_of(*ids), t % per_plane)
        return pl.BlockSpec((1, rows, TILE), index)
    return [spec(off, n, per) for off, n, per in DZ_ARRAYS]


def _dh_norm1_bwd(dz, w_in_t, x, g1, dx1):
    tm = 1024

    def body(*refs):
        seg_refs = refs[:3]
        w_ref, x_ref, g_ref, dx1_ref, gx_ref, dg_ref, acc = refs[3:]
        i, kk = pl.program_id(0), pl.program_id(1)

        @pl.when(kk == 0)
        def _():
            acc[...] = jnp.zeros_like(acc)

        for s, (off, n, _) in enumerate(DZ_ARRAYS):
            @pl.when((kk >= off) & (kk < off + n))
            def _(s=s):
                acc[...] += _dot(seg_refs[s][0], w_ref[...])

        @pl.when((i == 0) & (kk == 0))
        def _():
            dg_ref[...] = jnp.zeros_like(dg_ref)

        @pl.when(kk == N_DZ_TILES - 1)
        def _():
            xv = x_ref[...]
            r = lax.rsqrt(jnp.mean(xv * xv, axis=-1, keepdims=True) + EPS)
            xh = xv * r
            dh = acc[...]
            dg_ref[...] += jnp.sum(dh * xh, axis=0, keepdims=True)
            gx_ref[...] = dx1_ref[...] + _rms_bwd(dh, xh, r, g_ref[...])

    tok = pl.BlockSpec((tm, D), lambda i, j: (i, 0))
    vec = pl.BlockSpec((1, D), lambda i, j: (0, 0))
    return pl.pallas_call(
        body, name="dh_norm1_bwd",
        out_shape=(jax.ShapeDtypeStruct((T, D), F32), jax.ShapeDtypeStruct((1, D), F32)),
        grid=(T // tm, N_DZ_TILES),
        in_specs=_dz_specs(tm, lambda i, j: j, lambda i, j: i)
        + [pl.BlockSpec((TILE, D), lambda i, j: (j, 0)), tok, vec, tok],
        out_specs=(tok, vec),
        scratch_shapes=[pltpu.VMEM((tm, D), F32)],
        compiler_params=_params(dimension_semantics=("arbitrary", "arbitrary")),
    )(*dz, w_in_t, x, g1, dx1)


def _grad_w_in(dz, h):
    def body(*refs):
        seg_refs = refs[:3]
        h_ref, gw_ref, gb_ref = refs[3:]
        j = pl.program_id(0)

        for s, (off, n, _) in enumerate(DZ_ARRAYS):
            @pl.when((j >= off) & (j < off + n))
            def _(s=s):
                a = seg_refs[s][0]
                gw_ref[...] = _dot_tn(a, h_ref[...]).astype(BF16)
                gb_ref[...] = jnp.sum(a.astype(F32), axis=0, keepdims=True)

    return pl.pallas_call(
        body, name="grad_w_in",
        out_shape=(jax.ShapeDtypeStruct((D_IN, D), BF16), jax.ShapeDtypeStruct((1, D_IN), F32)),
        grid=(N_DZ_TILES,),
        in_specs=_dz_specs(T, lambda j: j, lambda j: 0) + [pl.BlockSpec((T, D), lambda j: (0, 0))],
        out_specs=(pl.BlockSpec((TILE, D), lambda j: (j, 0)), pl.BlockSpec((1, TILE), lambda j: (0, j))),
        compiler_params=_params(dimension_semantics=("parallel",)),
    )(*dz, h)


def _rpb_rows(rpb):
    padded = jnp.pad(rpb, ((0, 0), (0, 0), (0, GRID_W - N_RPB_C)))
    rows = [padded[:, WIN_H - 1 - oi: 2 * WIN_H - 1 - oi].reshape(N_HEADS // 2, 2, KWIN)
            for oi in range(WIN_H)]
    return jnp.stack(rows, axis=0)


SKEW = KWIN - (WIN_W - 1)


MASKED = -1e30


def _bias_tiles(rows_ref, valid, bias_s):
    for oi in range(WIN_H):
        for hh in range(2):
            row = jnp.broadcast_to(rows_ref[oi, 0, hh:hh + 1, :], (GRID_W, KWIN))
            tile = pltpu.roll(row, SKEW, 1, stride=1, stride_axis=0)
            bias_s[oi, hh * GRID_W:(hh + 1) * GRID_W, :] = jnp.where(valid[:GRID_W], tile, MASKED)


def _bias_tile_grads(gb_s, flip, out_ref):
    for oi in range(WIN_H):
        for hh in range(2):
            g = _dot_exact(flip, gb_s[oi, hh * GRID_W:(hh + 1) * GRID_W, :])
            back = pltpu.roll(g, KWIN - (GRID_W - WIN_W), 1, stride=1, stride_axis=0)
            out_ref[0, oi, hh:hh + 1, :] = jnp.sum(back, axis=0, keepdims=True)


def _rpb_fold(row_grads):
    g = row_grads.transpose(1, 0, 2, 3).reshape(WIN_H, N_HEADS, WIN_H, GRID_W)
    g = g.transpose(0, 2, 1, 3)

    def body(g_ref, o_ref):
        for dr in range(N_RPB_R):
            terms = [g_ref[oi, i] for oi in range(WIN_H) for i in range(WIN_H) if i - oi + WIN_H - 1 == dr]
            acc = terms[0]
            for term in terms[1:]:
                acc = acc + term
            o_ref[dr] = acc

    out = pl.pallas_call(
        body, name="rpb_fold",
        out_shape=jax.ShapeDtypeStruct((N_RPB_R, N_HEADS, GRID_W), F32),
    )(g)
    return out.transpose(1, 0, 2)[:, :, :N_RPB_C]


def _att_scores(q_ref, k_ref, bias_ref, hmask, r):
    rs = jnp.clip(r - WIN_H // 2, 0, ROWS - WIN_H)
    oi = r - rs
    q0 = pl.multiple_of(r * GRID_W, GRID_W)
    k0 = pl.multiple_of(rs * GRID_W, GRID_W)
    q_r = q_ref[pl.ds(q0, GRID_W), :] * (DH ** -0.5)
    q2 = jnp.where(hmask, jnp.concatenate([q_r, q_r], axis=0), jnp.zeros((), BF16))
    kw = k_ref[pl.ds(k0, KWIN), :]
    s = _dot_nt(q2, kw) + bias_ref[oi]
    e = jnp.exp(s - jnp.max(s, axis=-1, keepdims=True))
    return e, 1.0 / jnp.sum(e, axis=-1, keepdims=True), q2, kw, q0, k0, oi


def _att_fwd(qkv, bias_rows):
    valid_np, hmask_np = _att_tables()

    def body(q_ref, k_ref, v_ref, rows_ref, valid_ref, hmask_ref, o_ref, bias_s):
        valid = valid_ref[...] > 0.5
        hmask = hmask_ref[...] > 0.5
        first_head = lax.broadcasted_iota(jnp.int32, (GRID_W, 2 * DH), 1) < DH
        _bias_tiles(rows_ref, valid, bias_s)

        def row(r, carry):
            e, rl, _, _, q0, k0, _ = _att_scores(q_ref, k_ref, bias_s, hmask, r)
            o2 = _dot((e * rl).astype(BF16), v_ref[pl.ds(k0, KWIN), :])
            o_ref[pl.ds(q0, GRID_W), :] = jnp.where(first_head, o2[:GRID_W], o2[GRID_W:]).astype(BF16)
            return carry

        lax.fori_loop(0, ROWS, row, 0, unroll=4)

    col = lambda off: pl.BlockSpec((T, 2 * DH), lambda hp: (0, hp + off))
    return pl.pallas_call(
        body, name="att_fwd",
        out_shape=jax.ShapeDtypeStruct((T, D_ATT), BF16),
        grid=(N_HEADS // 2,),
        in_specs=[col(0), col(4), col(8),
                  pl.BlockSpec((WIN_H, 1, 2, KWIN), lambda hp: (0, hp, 0, 0)),
                  pl.BlockSpec((2 * GRID_W, KWIN), lambda hp: (0, 0)),
                  pl.BlockSpec((2 * DH, 2 * DH), lambda hp: (0, 0))],
        out_specs=pl.BlockSpec((T, 2 * DH), lambda hp: (0, hp)),
        scratch_shapes=[pltpu.VMEM((WIN_H, 2 * GRID_W, KWIN), F32)],
        compiler_params=_params(dimension_semantics=("parallel",)),
    )(qkv, qkv, qkv, bias_rows, jnp.asarray(valid_np), jnp.asarray(hmask_np))


def _att_bwd(qkv, bias_rows, datt, after):
    valid_np, hmask_np = _att_tables()

    def body(q_ref, k_ref, v_ref, do_ref, rows_ref, valid_ref, hmask_ref, flip_ref,
             dqkv_ref, grows_ref, dk_acc, dv_acc, bias_s, gb_s):
        valid = valid_ref[...] > 0.5
        hmask = hmask_ref[...] > 0.5
        first_head = lax.broadcasted_iota(jnp.int32, (GRID_W, 2 * DH), 1) < DH
        dk_acc[...] = jnp.zeros_like(dk_acc)
        dv_acc[...] = jnp.zeros_like(dv_acc)
        gb_s[...] = jnp.zeros_like(gb_s)
        _bias_tiles(rows_ref, valid, bias_s)

        def row(r, carry):
            e, rl, q2, kw, q0, k0, oi = _att_scores(q_ref, k_ref, bias_s, hmask, r)
            do_r = do_ref[pl.ds(q0, GRID_W), :]
            do2 = jnp.where(hmask, jnp.concatenate([do_r, do_r], axis=0), jnp.zeros((), BF16))
            vw = v_ref[pl.ds(k0, KWIN), :]
            p = e * rl
            dp = _dot_nt(do2, vw)
            ds = p * (dp - jnp.sum(dp * p, axis=-1, keepdims=True))
            p16 = p.astype(BF16)
            ds16 = ds.astype(BF16)
            dv_acc[pl.ds(k0, KWIN), :] += _dot_tn(p16, do2)
            dk_acc[pl.ds(k0, KWIN), :] += _dot_tn(ds16, q2)
            dq2 = _dot(ds16, kw) * (DH ** -0.5)
            dqkv_ref[0, pl.ds(q0, GRID_W), :] = jnp.where(first_head, dq2[:GRID_W], dq2[GRID_W:]).astype(BF16)
            gb_s[oi] += ds
            return carry

        lax.fori_loop(0, ROWS, row, 0, unroll=4)
        dqkv_ref[1] = dk_acc[...].astype(BF16)
        dqkv_ref[2] = dv_acc[...].astype(BF16)
        _bias_tile_grads(gb_s, flip_ref[...], grows_ref)

    col = lambda off: pl.BlockSpec((T, 2 * DH), lambda hp: (0, hp + off))
    tiles = pltpu.VMEM((WIN_H, 2 * GRID_W, KWIN), F32)
    return pl.pallas_call(
        body, name="att_bwd",
        out_shape=(jax.ShapeDtypeStruct((3, T, D_ATT), BF16),
                   jax.ShapeDtypeStruct((N_HEADS // 2, WIN_H, 2, KWIN), F32)),
        grid=(N_HEADS // 2,),
        in_specs=[col(0), col(4), col(8), col(0),
                  pl.BlockSpec((WIN_H, 1, 2, KWIN), lambda hp: (0, hp, 0, 0)),
                  pl.BlockSpec((2 * GRID_W, KWIN), lambda hp: (0, 0)),
                  pl.BlockSpec((2 * DH, 2 * DH), lambda hp: (0, 0)),
                  pl.BlockSpec((GRID_W, GRID_W), lambda hp: (0, 0))],
        out_specs=(pl.BlockSpec((3, T, 2 * DH), lambda hp: (0, 0, hp)),
                   pl.BlockSpec((1, WIN_H, 2, KWIN), lambda hp: (hp, 0, 0, 0))),
        scratch_shapes=[pltpu.VMEM((T, 2 * DH), F32), pltpu.VMEM((T, 2 * DH), F32), tiles, tiles],
        compiler_params=_params(dimension_semantics=("parallel",)),
    )(qkv, qkv, qkv, datt, bias_rows, jnp.asarray(valid_np) + after, jnp.asarray(hmask_np),
      jnp.asarray(np.eye(GRID_W, dtype=np.float32)[::-1].copy()))


def _conv_taps(up):
    return (_shift_rows(up, 2), _shift_rows(up, 1), up, _shift_rows(up, -1))


def _pair_block_diag(w_pair, dup, same_half):
    return jnp.where(same_half, _dot(w_pair.astype(BF16), dup), 0.0).astype(BF16)


def _gates(u, u16, wa, ba, wi, bi, lam):
    r = _sigmoid(_dot(u16, wa) + ba)
    ig = _sigmoid(_dot(u16, wi) + bi)
    sp = _softplus(-lam)
    log_a = (-LRU_C) * r * sp
    a = jnp.exp(log_a)
    mult2 = jnp.maximum(_one_minus_square(log_a, a), 0.0)
    return r, ig, sp, a, jnp.sqrt(mult2), mult2


SCAN_BLOCKS = 2


def _scans(jobs):
    c = jobs[0][0].shape[1]
    nblk = T // 8
    rows = lax.broadcasted_iota(jnp.int32, (8, c), 0)

    def block(a, b, reverse):
        for s in (1, 2, 4):
            if reverse:
                keep = rows < 8 - s
                a_s = jnp.where(keep, pltpu.roll(a, 8 - s, 0), 1.0)
                b_s = jnp.where(keep, pltpu.roll(b, 8 - s, 0), 0.0)
            else:
                keep = rows >= s
                a_s = jnp.where(keep, pltpu.roll(a, s, 0), 1.0)
                b_s = jnp.where(keep, pltpu.roll(b, s, 0), 0.0)
            b = a * b_s + b
            a = a * a_s
        return a, b

    def step(i, carry):
        out = []
        for (a_ref, b_ref, h_ref, reverse), h_prev in zip(jobs, carry):
            for u in range(SCAN_BLOCKS):
                blk = i * SCAN_BLOCKS + u
                if reverse:
                    blk = nblk - 1 - blk
                t0 = pl.multiple_of(blk * 8, 8)
                a, b = block(a_ref[pl.ds(t0, 8), :], b_ref[pl.ds(t0, 8), :], reverse)
                h = a * h_prev + b
                h_ref[pl.ds(t0, 8), :] = h
                h_prev = jnp.broadcast_to(h[0:1] if reverse else h[7:8], (8, c))
            out.append(h_prev)
        return tuple(out)

    lax.fori_loop(0, nblk // SCAN_BLOCKS, step, tuple(jnp.zeros((8, c), F32) for _ in jobs))


def _rec_specs():
    tok = lambda off: pl.BlockSpec((T, CG), lambda g: (0, g + off))
    per_ch = lambda rows: pl.BlockSpec((rows, CG), lambda g: (0, g))
    wspec = pl.BlockSpec((2, 1, CG, REC_BLOCK), lambda g: (0, g, 0, 0))
    const = lambda shape: pl.BlockSpec(shape, lambda g: (0, 0))
    return tok, per_ch, wspec, const


def _rec_fwd(uy, conv_w, conv_b, w_a, b_a, w_i, b_i, lam):
    tok, per_ch, wspec, const = _rec_specs()

    def body(up_ref, yb_ref, cw_ref, cb_ref, wa_ref, ba_ref, wi_ref, bi_ref, lam_ref, dup_ref, half_ref,
             hf_ref, hb_ref, yrec_ref, a_f, bx_f, a_b, bx_b):
        dup = dup_ref[...]
        same_half = half_ref[...] > 0.5
        taps = _conv_taps(up_ref[...])
        u = cb_ref[...]
        for j in range(4):
            u = u + taps[j] * cw_ref[j:j + 1, :]
        u16 = u.astype(BF16)
        for d, (a_s, bx_s) in enumerate(((a_f, bx_f), (a_b, bx_b))):
            wa = _pair_block_diag(wa_ref[d, 0], dup, same_half)
            wi = _pair_block_diag(wi_ref[d, 0], dup, same_half)
            _, ig, _, a, mult, _ = _gates(u, u16, wa, ba_ref[d:d + 1, :], wi, bi_ref[d:d + 1, :],
                                       lam_ref[d:d + 1, :])
            a_s[...] = a
            bx_s[...] = mult * (ig * u)
        _scans([(a_f, bx_f, hf_ref, False), (a_b, bx_b, hb_ref, True)])
        gelu, _ = _gelu_and_grad(yb_ref[...])
        yrec_ref[...] = ((hf_ref[...] + hb_ref[...]) * gelu).astype(BF16)

    return pl.pallas_call(
        body, name="rec_fwd",
        out_shape=(jax.ShapeDtypeStruct((T, D_REC), F32), jax.ShapeDtypeStruct((T, D_REC), F32),
                   jax.ShapeDtypeStruct((T, D_REC), BF16)),
        grid=(N_CG,),
        in_specs=[tok(0), tok(N_CG), per_ch(4), per_ch(1), wspec, per_ch(2), wspec, per_ch(2), per_ch(2),
                  const((REC_BLOCK, CG)), const((CG, CG))],
        out_specs=(tok(0), tok(0), tok(0)),
        scratch_shapes=[pltpu.VMEM((T, CG), F32)] * 4,
        compiler_params=_params(dimension_semantics=("parallel",)),
    )(uy, uy, conv_w, conv_b, w_a, b_a, w_i, b_i, lam,
      jnp.asarray(_dup_table(), BF16), jnp.asarray(_pair_mask()))


def _rec_bwd(uy, hf, hb, dyrec, conv_w, conv_b, w_a, b_a, w_i, b_i, lam):
    tok, per_ch, wspec, const = _rec_specs()

    def body(up_ref, yb_ref, hf_ref, hb_ref, dy_ref, cw_ref, cb_ref, wa_ref, ba_ref, wi_ref, bi_ref,
             lam_ref, dup_ref, dupt_ref, half_ref,
             duy_ref, dcw_ref, dcb_ref, dwa_ref, dba_ref, dwi_ref, dbi_ref, dlam_ref,
             a_s0, a_s1, dh_s, g_s0, g_s1):
        dup = dup_ref[...]
        dup_t = dupt_ref[...]
        same_half = half_ref[...] > 0.5
        taps = _conv_taps(up_ref[...])
        u = cb_ref[...]
        for j in range(4):
            u = u + taps[j] * cw_ref[j:j + 1, :]
        u16 = u.astype(BF16)
        gelu, dgelu = _gelu_and_grad(yb_ref[...])
        dy = dy_ref[...]
        duy_ref[1] = (dy * (hf_ref[...] + hb_ref[...]) * dgelu).astype(BF16)
        dh_s[...] = dy * gelu
        gate_values = []
        for d, a_s in enumerate((a_s0, a_s1)):
            wa = _pair_block_diag(wa_ref[d, 0], dup, same_half)
            wi = _pair_block_diag(wi_ref[d, 0], dup, same_half)
            lam_d = lam_ref[d:d + 1, :]
            r, ig, sp, a, mult, mult2 = _gates(u, u16, wa, ba_ref[d:d + 1, :], wi, bi_ref[d:d + 1, :], lam_d)
            a_s[...] = _shift_rows(a, 1 if d == 1 else -1)
            gate_values.append((wa, wi, lam_d, r, ig, sp, a, mult, mult2))
        _scans([(a_s0, dh_s, g_s0, True), (a_s1, dh_s, g_s1, False)])
        du = jnp.zeros((T, CG), F32)
        for d, g_s in enumerate((g_s0, g_s1)):
            reverse = d == 1
            wa, wi, lam_d, r, ig, sp, a, mult, mult2 = gate_values[d]
            g = g_s[...]
            h_prev = _shift_rows(hb_ref[...], -1) if reverse else _shift_rows(hf_ref[...], 1)
            da = g * h_prev
            dmult = g * (ig * u)
            dig = g * mult * u
            du = du + g * mult * ig
            dmult_dlog = jnp.where(mult2 > 0.0, -(a * a) * lax.rsqrt(mult2), 0.0)
            dlog_a = da * a + dmult * dmult_dlog
            dr = dlog_a * ((-LRU_C) * sp)
            dsp = jnp.sum(dlog_a * ((-LRU_C) * r), axis=0, keepdims=True)
            dlam_ref[d:d + 1, :] = dsp * (-_sigmoid(-lam_d))
            dga = dr * r * (1.0 - r)
            dgi = dig * ig * (1.0 - ig)
            dga16 = dga.astype(BF16)
            dgi16 = dgi.astype(BF16)
            du = du + _dot_nt(dga16, wa) + _dot_nt(dgi16, wi)
            dwa_ref[d, 0] = _dot_exact(jnp.where(same_half, _dot_tn(u16, dga16), 0.0), dup_t)
            dwi_ref[d, 0] = _dot_exact(jnp.where(same_half, _dot_tn(u16, dgi16), 0.0), dup_t)
            dba_ref[d:d + 1, :] = jnp.sum(dga, axis=0, keepdims=True)
            dbi_ref[d:d + 1, :] = jnp.sum(dgi, axis=0, keepdims=True)
        dcb_ref[...] = jnp.sum(du, axis=0, keepdims=True)
        for j in range(4):
            dcw_ref[j:j + 1, :] = jnp.sum(du * taps[j], axis=0, keepdims=True)
        dup_in = (_shift_rows(du, -2) * cw_ref[0:1, :] + _shift_rows(du, -1) * cw_ref[1:2, :]
                  + du * cw_ref[2:3, :] + _shift_rows(du, 1) * cw_ref[3:4, :])
        duy_ref[0] = dup_in.astype(BF16)

    wshape = jax.ShapeDtypeStruct((2, N_CG, CG, REC_BLOCK), F32)
    vec = lambda rows: jax.ShapeDtypeStruct((rows, D_REC), F32)
    dup_np = _dup_table()
    return pl.pallas_call(
        body, name="rec_bwd",
        out_shape=(jax.ShapeDtypeStruct((2, T, D_REC), BF16),
                   vec(4), vec(1), wshape, vec(2), wshape, vec(2), vec(2)),
        grid=(N_CG,),
        in_specs=[tok(0), tok(N_CG), tok(0), tok(0), tok(0),
                  per_ch(4), per_ch(1), wspec, per_ch(2), wspec, per_ch(2), per_ch(2),
                  const((REC_BLOCK, CG)), const((CG, REC_BLOCK)), const((CG, CG))],
        out_specs=(pl.BlockSpec((2, T, CG), lambda g: (0, 0, g)),
                   per_ch(4), per_ch(1), wspec, per_ch(2), wspec, per_ch(2), per_ch(2)),
        scratch_shapes=[pltpu.VMEM((T, CG), F32)] * 5,
        compiler_params=_params(dimension_semantics=("parallel",)),
    )(uy, uy, hf, hb, dyrec, conv_w, conv_b, w_a, b_a, w_i, b_i, lam,
      jnp.asarray(dup_np, BF16), jnp.asarray(dup_np.T.copy()), jnp.asarray(_pair_mask()))


TM_MIX = 256


def _mix_specs():
    tok = lambda width, blk=0: pl.BlockSpec((TM_MIX, width), lambda i: (i, blk))
    full = lambda shape: pl.BlockSpec(shape, lambda i: (0, 0))
    return tok, full


def _mix_fwd(x, att, yrec, gg, w_att_o_t, w_rec_o, w_out):
    tok, full = _mix_specs()

    def body(x_ref, att_ref, yr_ref, ga_ref, gr_ref, wao_ref, wro_ref, wo_ref, x1_ref, mixed_ref):
        y_att = _dot_nt(att_ref[...], wao_ref[...])
        y_rec = _dot(yr_ref[...], wro_ref[...])
        mixed = (_sigmoid(ga_ref[...]) * y_att + _sigmoid(gr_ref[...]) * y_rec).astype(BF16)
        mixed_ref[...] = mixed
        x1_ref[...] = x_ref[...] + _dot(mixed, wo_ref[...])

    return pl.pallas_call(
        body, name="mix_fwd",
        out_shape=(jax.ShapeDtypeStruct((T, D), F32), jax.ShapeDtypeStruct((T, D), BF16)),
        grid=(T // TM_MIX,),
        in_specs=[tok(D), tok(D_ATT), tok(D_REC), tok(D, 0), tok(D, 1),
                  full((D, D_ATT)), full((D_REC, D)), full((D, D))],
        out_specs=(tok(D), tok(D)),
        compiler_params=_params(dimension_semantics=("parallel",)),
    )(x, att, yrec, gg, gg, w_att_o_t, w_rec_o, w_out)


def _mix_bwd(dx1, att, yrec, gg, w_att_o_t, w_rec_o, w_out):
    tok, full = _mix_specs()

    def body(dx_ref, att_ref, yr_ref, ga_ref, gr_ref, wao_ref, wro_ref, wo_ref,
             dgg_ref, dya_ref, dyr_ref, datt_ref, dyrp_ref):
        dmixed = _dot_nt(dx_ref[...].astype(BF16), wo_ref[...])
        y_att = _dot_nt(att_ref[...], wao_ref[...])
        y_rec = _dot(yr_ref[...], wro_ref[...])
        sa = _sigmoid(ga_ref[...])
        sr = _sigmoid(gr_ref[...])
        dgg_ref[0] = (dmixed * y_att * sa * (1.0 - sa)).astype(BF16)
        dgg_ref[1] = (dmixed * y_rec * sr * (1.0 - sr)).astype(BF16)
        dya = (dmixed * sa).astype(BF16)
        dyr = (dmixed * sr).astype(BF16)
        dya_ref[...] = dya
        dyr_ref[...] = dyr
        datt_ref[...] = _dot(dya, wao_ref[...]).astype(BF16)
        dyrp_ref[...] = _dot_nt(dyr, wro_ref[...])

    return pl.pallas_call(
        body, name="mix_bwd",
        out_shape=(jax.ShapeDtypeStruct((2, T, D), BF16),
                   jax.ShapeDtypeStruct((T, D), BF16), jax.ShapeDtypeStruct((T, D), BF16),
                   jax.ShapeDtypeStruct((T, D_ATT), BF16), jax.ShapeDtypeStruct((T, D_REC), F32)),
        grid=(T // TM_MIX,),
        in_specs=[tok(D), tok(D_ATT), tok(D_REC), tok(D, 0), tok(D, 1),
                  full((D, D_ATT)), full((D_REC, D)), full((D, D))],
        out_specs=(pl.BlockSpec((2, TM_MIX, D), lambda i: (0, i, 0)),
                   tok(D), tok(D), tok(D_ATT), tok(D_REC)),
        compiler_params=_params(dimension_semantics=("parallel",)),
    )(dx1, att, yrec, gg, gg, w_att_o_t, w_rec_o, w_out)


TM_FFN = 256
FF_CHUNK = 1024


def _ffn_loss(x1, target, g2, gf, w_ff1_t, w_ff2):
    n_chunks = D_FF // FF_CHUNK

    def body(x1_ref, tg_ref, g2_ref, gf_ref, w1_hbm, w2_hbm,
             loss_ref, dx1_ref, h2_ref, act_ref, dpre_ref, dx2_ref, dg2_ref, dgf_ref,
             w1, w2, relu_s):
        i = pl.program_id(0)

        @pl.when(i == 0)
        def _():
            pltpu.sync_copy(w1_hbm, w1)
            pltpu.sync_copy(w2_hbm, w2)
            loss_ref[...] = jnp.zeros_like(loss_ref)
            dg2_ref[...] = jnp.zeros_like(dg2_ref)
            dgf_ref[...] = jnp.zeros_like(dgf_ref)

        x1v = x1_ref[...]
        r2 = lax.rsqrt(jnp.mean(x1v * x1v, axis=-1, keepdims=True) + EPS)
        xh2 = x1v * r2
        h2 = (xh2 * g2_ref[...]).astype(BF16)
        h2_ref[...] = h2
        x2 = x1v
        for c in range(n_chunks):
            ff = slice(c * FF_CHUNK, (c + 1) * FF_CHUNK)
            rl = jnp.maximum(_dot_nt(h2, w1[ff, :]), 0.0)
            relu_s[:, ff] = rl
            act = (rl * rl).astype(BF16)
            act_ref[:, ff] = act
            x2 = x2 + _dot(act, w2[ff, :])
        r3 = lax.rsqrt(jnp.mean(x2 * x2, axis=-1, keepdims=True) + EPS)
        xh3 = x2 * r3
        err = xh3 * gf_ref[...] - tg_ref[...]
        loss_ref[...] += 0.5 * jnp.sum(jnp.mean(err * err, axis=-1, keepdims=True))
        dy = err * (1.0 / D)
        dgf_ref[...] += jnp.sum(dy * xh3, axis=0, keepdims=True)
        dx2 = _rms_bwd(dy, xh3, r3, gf_ref[...])
        dx2_16 = dx2.astype(BF16)
        dx2_ref[...] = dx2_16
        dh2 = jnp.zeros((TM_FFN, D), F32)
        for c in range(n_chunks):
            ff = slice(c * FF_CHUNK, (c + 1) * FF_CHUNK)
            dpre = (_dot_nt(dx2_16, w2[ff, :]) * (2.0 * relu_s[:, ff])).astype(BF16)
            dpre_ref[:, ff] = dpre
            dh2 = dh2 + _dot(dpre, w1[ff, :])
        dg2_ref[...] += jnp.sum(dh2 * xh2, axis=0, keepdims=True)
        dx1_ref[...] = dx2 + _rms_bwd(dh2, xh2, r2, g2_ref[...])

    tok = lambda width: pl.BlockSpec((TM_FFN, width), lambda i: (i, 0))
    vec = pl.BlockSpec((1, D), lambda i: (0, 0))
    hbm = pl.BlockSpec(memory_space=pl.ANY)
    return pl.pallas_call(
        body, name="ffn_loss",
        out_shape=(jax.ShapeDtypeStruct((8, 128), F32), jax.ShapeDtypeStruct((T, D), F32),
                   jax.ShapeDtypeStruct((T, D), BF16), jax.ShapeDtypeStruct((T, D_FF), BF16),
                   jax.ShapeDtypeStruct((T, D_FF), BF16), jax.ShapeDtypeStruct((T, D), BF16),
                   jax.ShapeDtypeStruct((1, D), F32), jax.ShapeDtypeStruct((1, D), F32)),
        grid=(T // TM_FFN,),
        in_specs=[tok(D), tok(D), vec, vec, hbm, hbm],
        out_specs=(pl.BlockSpec((8, 128), lambda i: (0, 0)), tok(D), tok(D), tok(D_FF), tok(D_FF), tok(D),
                   vec, vec),
        scratch_shapes=[pltpu.VMEM((D_FF, D), BF16), pltpu.VMEM((D_FF, D), BF16),
                        pltpu.VMEM((TM_FFN, D_FF), F32)],
        compiler_params=_params(dimension_semantics=("arbitrary",)),
    )(x1, target, g2, gf, w_ff1_t, w_ff2)


def _local_step(x, target, p, late_weights, reduce_early):
    bias = _rpb_rows(p["rpb"])
    pairs = lambda w: w.reshape(2, N_CG, CG, REC_BLOCK)
    w_a, w_i = pairs(p["w_rg_a"]), pairs(p["w_rg_i"])
    rec_params = (p["conv_w"], p["conv_b"], w_a, p["b_rg_a"], w_i, p["b_rg_i"], p["lru_lambda"])

    qkv, uy, gg, h = _in_proj(x, p["ln1_g"], p["w_in_t"], p["b_in"])
    att = _att_fwd(qkv, bias)
    hf, hb, yrec = _rec_fwd(uy, *rec_params)
    p = {**p, **late_weights(yrec)}
    x1, mixed = _mix_fwd(x, att, yrec, gg, p["w_att_o_t"], p["w_rec_o"], p["w_out"])
    loss8, dx1, h2, act, dpre, dx2, g_ln2, g_lnf = _ffn_loss(
        x1, target, p["ln2_g"], p["lnf_g"], p["w_ff1_t"], p["w_ff2"])

    dgg, dya, dyr, datt, dyrp = _mix_bwd(dx1, att, yrec, gg, p["w_att_o_t"], p["w_rec_o"], p["w_out"])
    duy, g_cw, g_cb, g_wa, g_ba, g_wi, g_bi, g_lam = _rec_bwd(uy, hf, hb, dyrp, *rec_params)
    blocks = lambda g: g.reshape(2, N_REC_BLOCKS, REC_BLOCK, REC_BLOCK)
    grads = {
        "w_att_o_t": _matmul(dya, att, "tn", BF16, "g_w_att_o"),
        "conv_w": g_cw, "conv_b": g_cb, "w_rg_a": blocks(g_wa), "b_rg_a": g_ba,
        "w_rg_i": blocks(g_wi), "b_rg_i": g_bi, "lru_lambda": g_lam,
        "w_rec_o": _matmul(yrec, dyr, "tn", BF16, "g_w_rec_o"),
        "w_out": _matmul(mixed, dx1, "tn", BF16, "g_w_out"),
        "ln2_g": g_ln2,
        "w_ff1_t": _matmul(dpre, h2, "tn", BF16, "g_w_ff1"),
        "w_ff2": _matmul(act, dx2, "tn", BF16, "g_w_ff2"),
        "lnf_g": g_lnf,
    }
    dqkv, gbias = _att_bwd(qkv, bias, datt, reduce_early(grads, None))
    dz = (dqkv, duy, dgg)
    grad_x, g_ln1 = _dh_norm1_bwd(dz, p["w_in_t"], x, p["ln1_g"] + reduce_early(None, dqkv), dx1)
    g_w_in_t, g_b_in = _grad_w_in(dz, h)
    grads.update(ln1_g=g_ln1, w_in_t=g_w_in_t, b_in=g_b_in, rpb=_rpb_fold(gbias))
    return loss8[0:1, 0:1], grad_x, grads


MESH_ID = pl.DeviceIdType.MESH
ANY = pl.BlockSpec(memory_space=pl.ANY)

CHAN_BLOCK_ROWS = 32
GATE_ROWS = 2 * 2 * N_REC_BLOCKS * REC_BLOCK * REC_BLOCK // (N_DEV * D)
SECTIONS = (("w_in_t", 704, D), ("w_rec_o", 128, D), ("w_out", 128, D), ("w_ff1_t", 512, D),
            ("w_ff2", 512, D), ("chan", CHAN_BLOCK_ROWS, D), ("w_att_o_t", 128, D_ATT),
            ("gates", GATE_ROWS, D))
N_SEC = len(SECTIONS)
N_CHAN_ROWS = 10
CHAN = (("conv_w", 4), ("b_rg_a", 2), ("b_rg_i", 2), ("lru_lambda", 2))


def _position():
    return lax.axis_index("x"), lax.axis_index("y"), lax.axis_index("c")


def _other_chips(x, y):
    return [(1 - x, y), (x, 1 - y), (1 - x, 1 - y)]


PASS_ON_ID, PAIR_EARLY_ID, PAIR_LATE_ID = 1, 2, 3


def _pair_handshake(x, y, c):
    barrier = pltpu.get_barrier_semaphore()
    pl.semaphore_signal(barrier, inc=1, device_id=(x, y, 1 - c), device_id_type=MESH_ID)
    pl.semaphore_wait(barrier, 1)


def _block_of(ref, dev, rows):
    return ref.at[pl.ds(pl.multiple_of(dev * rows, 16), rows)]


def _all_gather(shards, name):
    ns = len(shards)

    def body(*refs):
        x_refs, out_refs, done_ref = refs[:ns], refs[ns:2 * ns], refs[2 * ns]
        send_sems, recv_sems, local_sems = refs[2 * ns + 1:]
        done_ref[0, 0] = 0.0
        x, y, c = _position()
        me, sibling = (x, y, c), (x, y, 1 - c)
        x_nbr, y_nbr, diagonal = _other_chips(x, y)
        north = c == 1
        relay_from = (jnp.where(north, x_nbr[0], y_nbr[0]), jnp.where(north, x_nbr[1], y_nbr[1]))
        relay_to = (jnp.where(north, y_nbr[0], x_nbr[0]), jnp.where(north, y_nbr[1], x_nbr[1]))

        def rows(s, px, py, pc):
            return _block_of(out_refs[s], 4 * px + 2 * py + pc, shards[s].shape[0])

        def copy(k, s, block, to, from_shard=False):
            return pltpu.make_async_remote_copy(
                src_ref=x_refs[s] if from_shard else rows(s, *block), dst_ref=rows(s, *block),
                send_sem=send_sems.at[k * ns + s], recv_sem=recv_sems.at[k * ns + s],
                device_id=to, device_id_type=MESH_ID)

        sections = range(ns)
        mine = [pltpu.make_async_copy(x_refs[s], rows(s, *me), local_sems.at[s]) for s in sections]
        sent = [copy(k, s, me, to, True) for k, to in enumerate((sibling, (*x_nbr, c), (*y_nbr, c)))
                for s in sections]
        for cp in mine + sent:
            cp.start()
        for s in sections:
            copy(1, s, (*x_nbr, c), me).wait_recv()
            copy(2, s, (*y_nbr, c), me).wait_recv()
            sent += [copy(3, s, (*relay_from, c), (*relay_to, c)),
                     copy(4, s, (*x_nbr, c), sibling), copy(5, s, (*y_nbr, c), sibling)]
            for cp in sent[-3:]:
                cp.start()
        for s in sections:
            copy(3, s, (*diagonal, c), me).wait_recv()
            sent.append(copy(6, s, (*diagonal, c), sibling))
            sent[-1].start()
        for s in sections:
            copy(0, s, sibling, me).wait_recv()
            for k, chip in ((4, x_nbr), (5, y_nbr), (6, diagonal)):
                copy(k, s, (*chip, 1 - c), me).wait_recv()
        for cp in sent:
            cp.wait_send()
        for cp in mine:
            cp.wait()

    return pl.pallas_call(
        body, name=name,
        out_shape=tuple(jax.ShapeDtypeStruct((N_DEV * s.shape[0], s.shape[1]), s.dtype) for s in shards)
        + (jax.ShapeDtypeStruct((1, 1), F32),),
        in_specs=[ANY] * ns,
        out_specs=(ANY,) * ns + (pl.BlockSpec(memory_space=pltpu.SMEM),),
        scratch_shapes=[pltpu.SemaphoreType.DMA((7 * ns,)), pltpu.SemaphoreType.DMA((7 * ns,)),
                        pltpu.SemaphoreType.DMA((ns,))],
    )(*shards)


HBM = pl.BlockSpec(memory_space=pltpu.HBM)
SEM = pl.BlockSpec(memory_space=pltpu.SEMAPHORE)
EFFECT = pltpu.SideEffectType.DATAFLOW_SIDE_EFFECTING


def _in_hbm(a):
    return pltpu.with_memory_space_constraint(a, pltpu.HBM)


def _first_hop_copies(shards, x_refs, zones, send_sems, recv_sems):
    ns = len(shards)
    x, y, c = _position()
    targets = [(x, y, 1 - c)] + [(cx, cy, c) for cx, cy in _other_chips(x, y)]
    return [pltpu.make_async_remote_copy(
        src_ref=x_refs[s], dst_ref=_block_of(zones[s], 4 * x + 2 * y + c, shards[s].shape[0]),
        send_sem=send_sems.at[k * ns + s], recv_sem=recv_sems.at[k * ns + s],
        device_id=to, device_id_type=MESH_ID)
        for k, to in enumerate(targets) for s in range(ns)]


def _after_all(arrays, name):
    def body(*refs):
        refs[-1][...] = jnp.zeros_like(refs[-1])

    return pl.pallas_call(
        body, name=name,
        out_shape=jax.ShapeDtypeStruct((8, LANES), F32),
        in_specs=[pl.BlockSpec(memory_space=pl.ANY)] * len(arrays),
        out_specs=pl.BlockSpec(memory_space=pltpu.VMEM),
    )(*arrays)


def _own_blocks_placed(shards, after):
    ns = len(shards)
    x, y, c = _position()
    me = jnp.reshape(4 * x + 2 * y + c, (1,)).astype(jnp.int32)
    shards = [*shards[:-1], shards[-1] + after.astype(shards[-1].dtype)]

    def body(me_ref, *refs):
        for s in range(ns):
            refs[ns + s][...] = refs[s][...]

    return pl.pallas_call(
        body, name="own_blocks_placed",
        out_shape=tuple(jax.ShapeDtypeStruct((N_DEV * s.shape[0], s.shape[1]), s.dtype) for s in shards),
        grid_spec=pltpu.PrefetchScalarGridSpec(
            num_scalar_prefetch=1, grid=(1,),
            in_specs=[pl.BlockSpec(s.shape, lambda i, me: (0, 0)) for s in shards],
            out_specs=tuple(pl.BlockSpec(s.shape, lambda i, me: (me[0], 0)) for s in shards)),
        compiler_params=_params(dimension_semantics=("arbitrary",)),
    )(me, *shards)


def _gather_start(shards, after, name):
    ns = len(shards)
    zones = _own_blocks_placed(shards, after)

    def body(*refs):
        for cp in _first_hop_copies(shards, refs[:ns], refs[ns:2 * ns], refs[2 * ns], refs[2 * ns + 1]):
            cp.start()
        refs[-1][...] = jnp.zeros_like(refs[-1])

    out = pl.pallas_call(
        body, name=name,
        out_shape=(pltpu.SemaphoreType.DMA((4 * ns,)), pltpu.SemaphoreType.DMA((4 * ns,)),
                   *[pltpu.HBM(a.shape, a.dtype) for a in (*shards, *zones)],
                   jax.ShapeDtypeStruct((8, LANES), F32)),
        in_specs=[HBM] * (2 * ns),
        out_specs=(SEM, SEM, *[HBM] * (2 * ns), pl.BlockSpec(memory_space=pltpu.VMEM)),
        input_output_aliases={i: 2 + i for i in range(2 * ns)},
        compiler_params=pltpu.CompilerParams(has_side_effects=EFFECT),
    )(*[_in_hbm(a) for a in shards], *[_in_hbm(a) for a in zones])
    return out[0], out[1], out[2:2 + ns], out[2 + ns:2 + 2 * ns], out[-1]


def _gather_wait(send_sems, recv_sems, shards, zones, after, name):
    ns = len(shards)

    def body(*refs):
        for cp in _first_hop_copies(shards, refs[:ns], refs[ns:2 * ns], refs[2 * ns], refs[2 * ns + 1]):
            cp.wait_send()
            cp.wait_recv()

    out = pl.pallas_call(
        body, name=name,
        out_shape=tuple(pltpu.HBM(a.shape, a.dtype) for a in (*shards, *zones)),
        in_specs=[HBM] * (2 * ns) + [SEM, SEM, ANY],
        out_specs=(HBM,) * (2 * ns),
        input_output_aliases={i: i for i in range(2 * ns)},
        compiler_params=pltpu.CompilerParams(has_side_effects=EFFECT),
    )(*shards, *zones, send_sems, recv_sems, after)
    return out[ns:]


def _gather_pass_on(rows, zones, name):
    ns = len(zones)

    def body(*refs):
        in_refs, out_refs = refs[:ns], refs[ns:2 * ns]
        send_sems, recv_sems = refs[2 * ns:]
        x, y, c = _position()
        _pair_handshake(x, y, c)
        copies = [pltpu.make_async_remote_copy(
            src_ref=_block_of(in_refs[s], 4 * cx + 2 * cy + c, rows[s]),
            dst_ref=_block_of(out_refs[s], 4 * cx + 2 * cy + c, rows[s]),
            send_sem=send_sems.at[j * ns + s], recv_sem=recv_sems.at[j * ns + s],
            device_id=(x, y, 1 - c), device_id_type=MESH_ID)
            for j, (cx, cy) in enumerate(_other_chips(x, y)) for s in range(ns)]
        for cp in copies:
            cp.start()
        for cp in copies:
            cp.wait_recv()
        for cp in copies:
            cp.wait_send()

    return pl.pallas_call(
        body, name=name,
        out_shape=tuple(jax.ShapeDtypeStruct(z.shape, z.dtype) for z in zones),
        in_specs=[ANY] * ns, out_specs=(ANY,) * ns,
        input_output_aliases={i: i for i in range(ns)},
        scratch_shapes=[pltpu.SemaphoreType.DMA((3 * ns,)), pltpu.SemaphoreType.DMA((3 * ns,))],
        compiler_params=pltpu.CompilerParams(collective_id=PASS_ON_ID),
    )(*zones)


def _pair_copies(sections, g_refs, land, send_sems, recv_sems):
    ns = len(sections)
    x, y, c = _position()
    return [pltpu.make_async_remote_copy(
        src_ref=_block_of(g_refs[s], 2 * k + 1 - c, rows), dst_ref=land[s].at[k],
        send_sem=send_sems.at[k * ns + s], recv_sem=recv_sems.at[k * ns + s],
        device_id=(x, y, 1 - c), device_id_type=MESH_ID)
        for k in range(N_CHIPS) for s, (_, rows, _) in enumerate(sections)]


def _pair_exchange_start(sections, grads, barrier_id, name):
    ns = len(sections)

    def body(*refs):
        _pair_handshake(*_position())
        for cp in _pair_copies(sections, refs[:ns], refs[ns:2 * ns], refs[2 * ns], refs[2 * ns + 1]):
            cp.start()
        refs[-1][...] = jnp.zeros_like(refs[-1])

    zones = [lax.empty((N_CHIPS, rows, cols), BF16) for _, rows, cols in sections]
    n = N_CHIPS * ns
    out = pl.pallas_call(
        body, name=name,
        out_shape=(pltpu.SemaphoreType.DMA((n,)), pltpu.SemaphoreType.DMA((n,)),
                   *[pltpu.HBM(a.shape, a.dtype) for a in (*grads, *zones)],
                   jax.ShapeDtypeStruct((8, LANES), F32)),
        in_specs=[HBM] * (2 * ns),
        out_specs=(SEM, SEM, *[HBM] * (2 * ns), pl.BlockSpec(memory_space=pltpu.VMEM)),
        input_output_aliases={i: 2 + i for i in range(2 * ns)},
        compiler_params=pltpu.CompilerParams(has_side_effects=EFFECT, collective_id=barrier_id),
    )(*[_in_hbm(a) for a in grads], *[_in_hbm(a) for a in zones])
    return out[0], out[1], out[2:2 + ns], out[2 + ns:2 + 2 * ns], out[-1]


def _pair_exchange_wait(sections, send_sems, recv_sems, grads, zones, after, name):
    ns = len(sections)

    def body(*refs):
        for cp in _pair_copies(sections, refs[:ns], refs[ns:2 * ns], refs[2 * ns], refs[2 * ns + 1]):
            cp.wait_send()
            cp.wait_recv()

    out = pl.pallas_call(
        body, name=name,
        out_shape=tuple(pltpu.HBM(a.shape, a.dtype) for a in (*grads, *zones)),
        in_specs=[HBM] * (2 * ns) + [SEM, SEM, ANY],
        out_specs=(HBM,) * (2 * ns),
        input_output_aliases={i: i for i in range(2 * ns)},
        compiler_params=pltpu.CompilerParams(has_side_effects=EFFECT),
    )(*grads, *zones, send_sems, recv_sems, after)
    return out[:ns], out[ns:]


def _pair_add(sections, grads, got, core, name):
    ns = len(sections)

    def body(core_ref, *refs):
        g_refs, got_refs, p_refs = refs[:ns], refs[ns:2 * ns], refs[2 * ns:]
        for s in range(ns):
            p_refs[s][0] = (g_refs[s][...].astype(F32) + got_refs[s][0].astype(F32)).astype(BF16)

    slot = [pl.BlockSpec((1, rows, cols), lambda k, c: (k, 0, 0)) for _, rows, cols in sections]
    return pl.pallas_call(
        body, name=name,
        out_shape=tuple(jax.ShapeDtypeStruct((N_CHIPS, rows, cols), BF16) for _, rows, cols in sections),
        grid_spec=pltpu.PrefetchScalarGridSpec(
            num_scalar_prefetch=1, grid=(N_CHIPS,),
            in_specs=[pl.BlockSpec((rows, cols), lambda k, c: (2 * k + c[0], 0)) for _, rows, cols in sections]
            + slot,
            out_specs=tuple(slot)),
        compiler_params=_params(dimension_semantics=("parallel",)),
    )(core, *grads, *got)


def _chip_copies(sections, p_refs, land, send_sems, recv_sems):
    ns = len(sections)
    x, y, c = _position()
    return [pltpu.make_async_remote_copy(
        src_ref=p_refs[s].at[2 * cx + cy], dst_ref=land[s].at[j],
        send_sem=send_sems.at[j * ns + s], recv_sem=recv_sems.at[j * ns + s],
        device_id=(cx, cy, c), device_id_type=MESH_ID)
        for j, (cx, cy) in enumerate(_other_chips(x, y)) for s in range(ns)]


def _chip_exchange(sections, parts, name):
    ns = len(sections)

    def body(*refs):
        copies = _chip_copies(sections, refs[:ns], refs[ns:2 * ns], *refs[2 * ns:])
        for cp in copies:
            cp.start()
        for cp in copies:
            cp.wait_recv()
        for cp in copies:
            cp.wait_send()

    n = 3 * ns
    return pl.pallas_call(
        body, name=name,
        out_shape=tuple(jax.ShapeDtypeStruct((3, rows, cols), BF16) for _, rows, cols in sections),
        in_specs=[ANY] * ns, out_specs=(ANY,) * ns,
        scratch_shapes=[pltpu.SemaphoreType.DMA((n,)), pltpu.SemaphoreType.DMA((n,))],
    )(*parts)


def _chip_exchange_start(sections, parts, name):
    ns = len(sections)

    def body(*refs):
        p_refs, land = refs[:ns], refs[ns:2 * ns]
        send_sems, recv_sems = refs[2 * ns], refs[2 * ns + 1]
        token = refs[-1]
        for cp in _chip_copies(sections, p_refs, land, send_sems, recv_sems):
            cp.start()
        token[...] = jnp.zeros_like(token)

    zones = [lax.empty((3, rows, cols), BF16) for _, rows, cols in sections]
    out = pl.pallas_call(
        body, name=name,
        out_shape=(pltpu.SemaphoreType.DMA((3 * ns,)), pltpu.SemaphoreType.DMA((3 * ns,)),
                   *[pltpu.HBM(a.shape, a.dtype) for a in parts], *[pltpu.HBM(a.shape, a.dtype) for a in zones],
                   jax.ShapeDtypeStruct((8, LANES), F32)),
        in_specs=[HBM] * (2 * ns),
        out_specs=(SEM, SEM, *[HBM] * (2 * ns), pl.BlockSpec(memory_space=pltpu.VMEM)),
        input_output_aliases={i: 2 + i for i in range(2 * ns)},
        compiler_params=pltpu.CompilerParams(has_side_effects=EFFECT),
    )(*[_in_hbm(a) for a in parts], *[_in_hbm(a) for a in zones])
    return out[0], out[1], out[2:2 + ns], out[2 + ns:2 + 2 * ns], out[-1]


def _chip_exchange_wait(sections, send_sems, recv_sems, parts, zones, after, name):
    ns = len(sections)

    def body(*refs):
        p_refs, land = refs[:ns], refs[ns:2 * ns]
        for cp in _chip_copies(sections, p_refs, land, refs[2 * ns], refs[2 * ns + 1]):
            cp.wait_send()
            cp.wait_recv()

    out = pl.pallas_call(
        body, name=name,
        out_shape=tuple(pltpu.HBM(a.shape, a.dtype) for a in (*parts, *zones)),
        in_specs=[HBM] * (2 * ns) + [SEM, SEM, ANY],
        out_specs=(HBM,) * (2 * ns),
        input_output_aliases={i: i for i in range(2 * ns)},
        compiler_params=pltpu.CompilerParams(has_side_effects=EFFECT),
    )(*parts, *zones, send_sems, recv_sems, after)
    return out[:ns], out[ns:]


def _grad_finish(sections, parts, far, chip, name):
    ns = len(sections)

    def body(chip_ref, *refs):
        p_refs, b_refs, g_refs = refs[:ns], refs[ns:2 * ns], refs[2 * ns:]
        for s in range(ns):
            g = p_refs[s][0].astype(F32)
            for j in range(3):
                g = g + b_refs[s][j].astype(F32)
            g_refs[s][...] = g

    half = [(rows // 2, cols) for _, rows, cols in sections]
    return pl.pallas_call(
        body, name=name,
        out_shape=tuple(jax.ShapeDtypeStruct((rows, cols), F32) for _, rows, cols in sections),
        grid_spec=pltpu.PrefetchScalarGridSpec(
            num_scalar_prefetch=1, grid=(2,),
            in_specs=[pl.BlockSpec((1, r, c), lambda i, chip: (chip[0], i, 0)) for r, c in half]
            + [pl.BlockSpec((3, r, c), lambda i, chip: (0, i, 0)) for r, c in half],
            out_specs=tuple(pl.BlockSpec((r, c), lambda i, chip: (i, 0)) for r, c in half)),
        compiler_params=_params(dimension_semantics=("parallel",)),
    )(chip, *parts, *far)


def _sum_devices(parts, rows, name):
    cols = parts.shape[1]
    tr = rows // 2

    def body(*refs):
        s = refs[0][...].astype(F32)
        for d in range(1, N_DEV):
            s = s + refs[d][...].astype(F32)
        refs[N_DEV][...] = s

    return pl.pallas_call(
        body, name=name,
        out_shape=jax.ShapeDtypeStruct((rows, cols), F32),
        grid=(2,),
        in_specs=[pl.BlockSpec((tr, cols), lambda i, d=d: (2 * d + i, 0)) for d in range(N_DEV)],
        out_specs=pl.BlockSpec((tr, cols), lambda i: (i, 0)),
        compiler_params=_params(dimension_semantics=("parallel",)),
    )(*([parts] * N_DEV))


def _adamw_step(w_ref, g_ref, m_ref, v_ref, d_ref, nm_ref, nv_ref):
    c1 = 1.0 / (1.0 - ADAM_B1 ** ADAM_STEP)
    c2 = 1.0 / (1.0 - ADAM_B2 ** ADAM_STEP)
    gv = g_ref[...]
    nm = ADAM_B1 * m_ref[...] + (1.0 - ADAM_B1) * gv
    nv = ADAM_B2 * v_ref[...] + (1.0 - ADAM_B2) * (gv * gv)
    nm_ref[...] = nm
    nv_ref[...] = nv
    d_ref[...] = (-ADAM_LR) * ((nm * c1) / (jnp.sqrt(nv * c2) + ADAM_EPS) + ADAM_WD * w_ref[...])


def _adamw_small(params, name):
    n = len(params)

    def body(*refs):
        for k in range(n):
            _adamw_step(*refs[4 * k:4 * k + 4], *refs[4 * n + 3 * k:4 * n + 3 * k + 3])

    out = pl.pallas_call(
        body, name=name,
        out_shape=tuple(jax.ShapeDtypeStruct(p[0].shape, F32) for p in params for _ in range(3)),
    )(*[a for p in params for a in p])
    return [out[3 * k:3 * k + 3] for k in range(n)]


def _adamw(w, g, m, v, name):
    rows, cols = w.shape
    tr = rows
    while tr * cols * 4 > (1 << 20) and tr % 16 == 0:
        tr //= 2

    def body(*refs):
        _adamw_step(*refs)

    spec = pl.BlockSpec((tr, cols), lambda i: (i, 0))
    shape = jax.ShapeDtypeStruct((rows, cols), F32)
    return pl.pallas_call(
        body, name=name,
        out_shape=(shape, shape, shape),
        grid=(rows // tr,),
        in_specs=[spec] * 4, out_specs=(spec,) * 3,
        compiler_params=_params(dimension_semantics=("parallel",)),
    )(w, g, m, v)


NAMES = ("ln1_g", "w_in", "b_in", "rpb", "w_att_o", "conv_w", "conv_b", "w_rg_a", "b_rg_a", "w_rg_i",
         "b_rg_i", "lru_lambda", "w_rec_o", "w_out", "ln2_g", "w_ff1", "w_ff2", "lnf_g")
TRANSPOSED = {"w_in": "w_in_t", "w_att_o": "w_att_o_t", "w_ff1": "w_ff1_t"}
ROW_SHARDED = ("w_rec_o", "w_out", "w_ff2")
REPLICATED = (("ln1_g", (1, D)), ("b_in", (1, D_IN)), ("rpb", (N_HEADS * N_RPB_R, N_RPB_C)),
              ("conv_b", (1, D_REC)), ("w_rg_a", (2 * N_REC_BLOCKS * REC_BLOCK, REC_BLOCK)),
              ("w_rg_i", (2 * N_REC_BLOCKS * REC_BLOCK, REC_BLOCK)), ("ln2_g", (1, D)), ("lnf_g", (1, D)))
GATE_BLOCKS = ("w_rg_a", "w_rg_i")
SMALL_ROWS = 112


def _chan_bits(vectors):
    chan = jnp.concatenate(vectors, axis=0)
    bits = lax.bitcast_convert_type(chan, BF16).reshape(-1)
    return jnp.pad(bits, (0, CHAN_BLOCK_ROWS * D - bits.shape[0])).reshape(CHAN_BLOCK_ROWS, D)


def _chan_from_bits(gathered):
    bits = gathered.reshape(N_DEV, CHAN_BLOCK_ROWS * D)[:, :2 * N_CHAN_ROWS * LANES]
    chan = lax.bitcast_convert_type(bits.reshape(N_DEV, N_CHAN_ROWS, LANES, 2), F32)
    return chan.transpose(1, 0, 2).reshape(N_CHAN_ROWS, D)


def kernel(x, ln1_g, w_in, b_in, rpb, w_att_o, conv_w, conv_b, w_rg_a, b_rg_a, w_rg_i, b_rg_i, lru_lambda, w_rec_o, w_out, ln2_g, w_ff1, w_ff2, lnf_g, loss_target, m_ln1_g, m_w_in, m_b_in, m_rpb, m_w_att_o, m_conv_w, m_conv_b, m_w_rg_a, m_b_rg_a, m_w_rg_i, m_b_rg_i, m_lru_lambda, m_w_rec_o, m_w_out, m_ln2_g, m_w_ff1, m_w_ff2, m_lnf_g, v_ln1_g, v_w_in, v_b_in, v_rpb, v_w_att_o, v_conv_w, v_conv_b, v_w_rg_a, v_b_rg_a, v_w_rg_i, v_b_rg_i, v_lru_lambda, v_w_rec_o, v_w_out, v_ln2_g, v_w_ff1, v_w_ff2, v_lnf_g):
    w = dict(zip(NAMES, (ln1_g, w_in, b_in, rpb, w_att_o, conv_w, conv_b, w_rg_a, b_rg_a, w_rg_i,
                         b_rg_i, lru_lambda, w_rec_o, w_out, ln2_g, w_ff1, w_ff2, lnf_g)))
    m = dict(zip(NAMES, (m_ln1_g, m_w_in, m_b_in, m_rpb, m_w_att_o, m_conv_w, m_conv_b, m_w_rg_a,
                         m_b_rg_a, m_w_rg_i, m_b_rg_i, m_lru_lambda, m_w_rec_o, m_w_out, m_ln2_g,
                         m_w_ff1, m_w_ff2, m_lnf_g)))
    v = dict(zip(NAMES, (v_ln1_g, v_w_in, v_b_in, v_rpb, v_w_att_o, v_conv_w, v_conv_b, v_w_rg_a,
                         v_b_rg_a, v_w_rg_i, v_b_rg_i, v_lru_lambda, v_w_rec_o, v_w_out, v_ln2_g,
                         v_w_ff1, v_w_ff2, v_lnf_g)))
    xi, yi, ci = _position()

    shard = {t: w[n][0].T.astype(BF16) for n, t in TRANSPOSED.items()}
    shard.update({n: w[n][0].astype(BF16) for n in ROW_SHARDED})
    shard["chan"] = _chan_bits([w[n][0] for n, _ in CHAN])
    first, later = ("w_in_t", "chan"), ("w_rec_o", "w_out", "w_att_o_t", "w_ff1_t", "w_ff2")
    *gathered, done = _all_gather([shard[n] for n in first], "weight_all_gather")
    p = dict(zip(first, gathered))
    send_sems, recv_sems, sent, zones, token = _gather_start([shard[n] for n in later], done,
                                                             "weight_gather_start")

    def late_weights(after):
        landed = _gather_wait(send_sems, recv_sems, sent, zones, after, "weight_gather_wait")
        return dict(zip(later, _gather_pass_on([shard[n].shape[0] for n in later], landed,
                                               "weight_gather_pass_on")))

    chan = _chan_from_bits(p.pop("chan"))
    r0 = 0
    for n, rows in CHAN:
        p[n] = chan[r0:r0 + rows]
        r0 += rows
    p.update(ln1_g=w["ln1_g"], b_in=w["b_in"] + token[0, 0], rpb=w["rpb"][0], conv_b=w["conv_b"],
             w_rg_a=w["w_rg_a"][0], w_rg_i=w["w_rg_i"][0], ln2_g=w["ln2_g"],
             lnf_g=w["lnf_g"].reshape(1, D))

    core = jnp.reshape(ci, (1,)).astype(jnp.int32)
    chip = jnp.reshape(2 * xi + yi, (1,)).astype(jnp.int32)
    early_sections, late_sections = SECTIONS[1:], SECTIONS[:1]
    in_flight = {}

    def pair_sum_and_send(group, sections, after):
        send_sems, recv_sems, sect, zones, _ = in_flight["pair_" + group]
        sect, got = _pair_exchange_wait(sections, send_sems, recv_sems, sect, zones, after,
                                        "grad_pair_exchange_wait_" + group)
        parts = _pair_add(sections, sect, got, core, "grad_pair_add_" + group)
        in_flight[group] = _chip_exchange_start(sections, parts, "grad_chip_exchange_start_" + group)
        return in_flight[group][-1]

    def reduce_early(grads, after):
        if grads is None:
            return pair_sum_and_send("early", early_sections, after)[0, 0]
        chan_g = jnp.concatenate([grads[n] for n, _ in CHAN], axis=0)
        chan_g = chan_g.reshape(N_CHAN_ROWS, N_DEV, LANES).transpose(1, 0, 2).astype(BF16)
        chan_g = jnp.pad(chan_g.reshape(N_DEV, -1), ((0, 0), (0, CHAN_BLOCK_ROWS * D - N_CHAN_ROWS * LANES)))
        grads["chan"] = chan_g.reshape(N_DEV * CHAN_BLOCK_ROWS, D)
        grads["gates"] = jnp.concatenate([grads[n].reshape(-1, D) for n in GATE_BLOCKS], axis=0).astype(BF16)
        in_flight["pair_early"] = _pair_exchange_start(
            early_sections, [grads[n] for n, _, _ in early_sections], PAIR_EARLY_ID,
            "grad_pair_exchange_start_early")
        return in_flight["pair_early"][-1][0, 0]

    loss_part, grad_x, grads = _local_step(x[0], loss_target[0], p, late_weights, reduce_early)
    in_flight["pair_late"] = _pair_exchange_start(
        late_sections, [grads[n] for n, _, _ in late_sections], PAIR_LATE_ID, "grad_pair_exchange_start_late")

    def finish(group, sections, after, name):
        send_sems, recv_sems, parts, zones, _ = in_flight[group]
        parts, far = _chip_exchange_wait(sections, send_sems, recv_sems, parts, zones, after,
                                         "grad_chip_exchange_wait_" + name)
        return dict(zip((n for n, _, _ in sections),
                        _grad_finish(sections, parts, far, chip, "grad_finish_" + name)))

    summed = finish("early", early_sections, in_flight["pair_late"][-1], "early")
    started_late = pair_sum_and_send("late", late_sections, summed["gates"])

    flat = jnp.concatenate([grads[n].reshape(-1) for n, _ in REPLICATED if n not in GATE_BLOCKS]
                           + [loss_part.reshape(-1) + started_late[0, 0]])
    n_small = flat.shape[0]
    flat = jnp.pad(flat, (0, SMALL_ROWS * LANES - n_small)).reshape(SMALL_ROWS, LANES)
    small_parts, gate_sum, _ = _all_gather([flat, summed["gates"]], "small_grad_all_gather")
    small = _sum_devices(small_parts, SMALL_ROWS, "small_grad_sum").reshape(-1)
    loss = small[n_small - 1]

    g, delta, new_m, new_v = {}, {}, {}, {}

    def update(n, g2, shape2):
        d2, m2, v2 = _adamw(w[n].reshape(shape2), g2, m[n].reshape(shape2), v[n].reshape(shape2),
                            "adamw_" + n)
        g[n], delta[n], new_m[n], new_v[n] = (a.reshape(w[n].shape) for a in (g2, d2, m2, v2))

    small_params = []
    o = 0
    for n, shape2 in REPLICATED:
        if n in GATE_BLOCKS:
            k, rows = GATE_BLOCKS.index(n), gate_sum.shape[0] // len(GATE_BLOCKS)
            update(n, gate_sum[k * rows:(k + 1) * rows].reshape(shape2), shape2)
        else:
            size = shape2[0] * shape2[1]
            small_params.append((n, small[o:o + size].reshape(shape2), shape2))
            o += size
    chan_back = summed["chan"].reshape(-1)[:N_CHAN_ROWS * LANES].reshape(N_CHAN_ROWS, LANES)
    r0 = 0
    for n, rows in CHAN:
        small_params.append((n, chan_back[r0:r0 + rows], (rows, LANES)))
        r0 += rows
    results = _adamw_small([(w[n].reshape(s2), g2, m[n].reshape(s2), v[n].reshape(s2))
                            for n, g2, s2 in small_params], "adamw_vectors")
    for (n, g2, _), (d2, m2, v2) in zip(small_params, results):
        g[n], delta[n], new_m[n], new_v[n] = (a.reshape(w[n].shape) for a in (g2, d2, m2, v2))

    for n in ROW_SHARDED:
        update(n, summed[n], summed[n].shape)
    for n, t in TRANSPOSED.items():
        if t in summed:
            update(n, summed[t].T, summed[t].shape[::-1])
    summed = finish("late", late_sections, _after_all(list(delta.values()), "updates_done"), "late")
    update("w_in", summed["w_in_t"].T, summed["w_in_t"].shape[::-1])

    return (loss, grad_x[None], *[g[n] for n in NAMES], *[delta[n] for n in NAMES],
            *[new_m[n] for n in NAMES], *[new_v[n] for n in NAMES])
```

```python
import math

import numpy as np
import jax
import jax.numpy as jnp
from jax import lax
from jax.experimental import pallas as pl
from jax.experimental.pallas import tpu as pltpu

F32 = jnp.float32
BF16 = jnp.bfloat16

T = 2048
D = 1024
D_ATT = 512
D_REC = 1024
D_FF = 4096
D_IN = 5632
N_HEADS = 8
DH = 64
GRID_W = 64
ROWS = T // GRID_W
WIN_H = 8
WIN_W = 16
KWIN = WIN_H * GRID_W
N_RPB_R = 2 * WIN_H - 1
N_RPB_C = 2 * WIN_W - 1
N_REC_BLOCKS = 16
REC_BLOCK = 64
CG = 128
N_CG = D_REC // CG
LRU_C = 8.0
EPS = 1e-6
N_DEV = 8
N_CHIPS = 4
LANES = 128

ADAM_LR = 0.001
ADAM_B1 = 0.9
ADAM_B2 = 0.999
ADAM_EPS = 1e-08
ADAM_WD = 0.01
ADAM_STEP = 10

MESH_AXES = ("x", "y", "c")
VMEM_LIMIT = 56 * 1024 * 1024

TILE = 512
DZ_ARRAYS = ((0, 3, 1), (3, 4, 2), (7, 4, 2))
N_DZ_TILES = D_IN // TILE


def _params(**kw):
    return pltpu.CompilerParams(vmem_limit_bytes=VMEM_LIMIT, **kw)


def _att_tables():
    rq = np.arange(2 * GRID_W) % GRID_W
    kc = np.arange(KWIN) % GRID_W
    win_start = np.clip(rq - WIN_W // 2, 0, GRID_W - WIN_W)
    valid = (kc[None, :] >= win_start[:, None]) & (kc[None, :] < win_start[:, None] + WIN_W)
    return valid.astype(np.float32), _pair_mask()


def _pair_mask():
    half = np.arange(2 * DH) // DH
    return (half[:, None] == half[None, :]).astype(np.float32)


def _dup_table():
    return np.concatenate([np.eye(REC_BLOCK, dtype=np.float32)] * 2, axis=1)


def _sigmoid(x):
    return 0.5 * jnp.tanh(0.5 * x) + 0.5


def _softplus(x):
    return jnp.maximum(x, 0.0) + jnp.log(1.0 + jnp.exp(-jnp.abs(x)))


def _one_minus_square(log_a, a):
    x = 2.0 * log_a
    series = -x * (1.0 + x * (0.5 + x * (1.0 / 6.0)))
    return jnp.where(x > -0.02, series, 1.0 - a * a)


_GELU_C = math.sqrt(2.0 / math.pi)


def _gelu_and_grad(x):
    x2 = x * x
    inner = _GELU_C * (x + 0.044715 * x * x2)
    t = jnp.tanh(inner)
    g = 0.5 * x * (1.0 + t)
    dg = 0.5 * (1.0 + t) + 0.5 * x * (1.0 - t * t) * _GELU_C * (1.0 + 3.0 * 0.044715 * x2)
    return g, dg


def _dot(a, b):
    return jnp.dot(a, b, preferred_element_type=F32)


def _dot_nt(a, b):
    return lax.dot_general(a, b, (((1,), (1,)), ((), ())), preferred_element_type=F32)


def _dot_tn(a, b):
    return lax.dot_general(a, b, (((0,), (0,)), ((), ())), preferred_element_type=F32)


def _dot_exact(a, b):
    return jnp.dot(a, b, precision=lax.Precision.HIGHEST, preferred_element_type=F32)


def _shift_rows(x, s):
    n = x.shape[0]
    rows = lax.broadcasted_iota(jnp.int32, x.shape, 0)
    y = pltpu.roll(x, s % n, 0)
    if s > 0:
        return jnp.where(rows >= s, y, 0.0)
    return jnp.where(rows < n + s, y, 0.0)


def _rms_bwd(dh, xh, r, g):
    dxh = dh * g
    return r * (dxh - xh * jnp.mean(dxh * xh, axis=-1, keepdims=True))


def _matmul(a, b, mode, out_dtype, name, tm=512, tn=1024, tk=2048):
    if mode == "nn":
        (m, k), (k2, n) = a.shape, b.shape
    elif mode == "nt":
        (m, k), (n, k2) = a.shape, b.shape
    else:
        (k, m), (k2, n) = a.shape, b.shape
    assert k == k2
    tm, tn, tk = min(tm, m), min(tn, n), min(tk, k)
    assert m % tm == 0 and n % tn == 0 and k % tk == 0
    nk = k // tk
    dot = {"nn": _dot, "nt": _dot_nt, "tn": _dot_tn}[mode]

    def body(a_ref, b_ref, o_ref, acc):
        kk = pl.program_id(2)
        part = dot(a_ref[...].astype(BF16), b_ref[...].astype(BF16))
        if nk == 1:
            o_ref[...] = part.astype(out_dtype)
            return

        @pl.when(kk == 0)
        def _():
            acc[...] = part

        @pl.when(kk > 0)
        def _():
            acc[...] += part

        @pl.when(kk == nk - 1)
        def _():
            o_ref[...] = acc[...].astype(out_dtype)

    if mode == "tn":
        a_spec = pl.BlockSpec((tk, tm), lambda i, j, kk: (kk, i))
    else:
        a_spec = pl.BlockSpec((tm, tk), lambda i, j, kk: (i, kk))
    if mode == "nt":
        b_spec = pl.BlockSpec((tn, tk), lambda i, j, kk: (j, kk))
    else:
        b_spec = pl.BlockSpec((tk, tn), lambda i, j, kk: (kk, j))
    return pl.pallas_call(
        body, name=name,
        out_shape=jax.ShapeDtypeStruct((m, n), out_dtype),
        grid=(m // tm, n // tn, nk),
        in_specs=[a_spec, b_spec],
        out_specs=pl.BlockSpec((tm, tn), lambda i, j, kk: (i, j)),
        scratch_shapes=[pltpu.VMEM((tm, tn) if nk > 1 else (8, LANES), F32)],
        compiler_params=_params(dimension_semantics=("parallel", "parallel", "arbitrary")),
    )(a, b)


def _in_proj(x, g1, w_in_t, b_in):
    tm = 1024

    def body(x_ref, g_ref, w_ref, b_ref, qkv_ref, uy_ref, gg_ref, h_ref, h_scr):
        j = pl.program_id(1)

        @pl.when(j == 0)
        def _():
            xv = x_ref[...]
            r = lax.rsqrt(jnp.mean(xv * xv, axis=-1, keepdims=True) + EPS)
            h = ((xv * r) * g_ref[...]).astype(BF16)
            h_scr[...] = h
            h_ref[...] = h

        z = _dot_nt(h_scr[...], w_ref[...]) + b_ref[...]

        @pl.when(j < 3)
        def _():
            qkv_ref[...] = z.astype(BF16)

        @pl.when((j >= 3) & (j < 7))
        def _():
            uy_ref[...] = z

        @pl.when(j >= 7)
        def _():
            gg_ref[...] = z

    return pl.pallas_call(
        body, name="in_proj",
        out_shape=(jax.ShapeDtypeStruct((T, 3 * D_ATT), BF16),
                   jax.ShapeDtypeStruct((T, 2 * D_REC), F32),
                   jax.ShapeDtypeStruct((T, 2 * D), F32),
                   jax.ShapeDtypeStruct((T, D), BF16)),
        grid=(T // tm, N_DZ_TILES),
        in_specs=[pl.BlockSpec((tm, D), lambda i, j: (i, 0)),
                  pl.BlockSpec((1, D), lambda i, j: (0, 0)),
                  pl.BlockSpec((TILE, D), lambda i, j: (j, 0)),
                  pl.BlockSpec((1, TILE), lambda i, j: (0, j))],
        out_specs=(pl.BlockSpec((tm, TILE), lambda i, j: (i, jnp.minimum(j, 2))),
                   pl.BlockSpec((tm, TILE), lambda i, j: (i, jnp.clip(j - 3, 0, 3))),
                   pl.BlockSpec((tm, TILE), lambda i, j: (i, jnp.clip(j - 7, 0, 3))),
                   pl.BlockSpec((tm, D), lambda i, j: (i, 0))),
        scratch_shapes=[pltpu.VMEM((tm, D), BF16)],
        compiler_params=_params(dimension_semantics=("parallel", "arbitrary")),
    )(x, g1, w_in_t, b_in)


def _dz_specs(rows, tile_of, row_of):
    def spec(off, n, per_plane):
        def index(*ids):
            t = jnp.clip(tile_of(*ids) - off, 0, n - 1)
            return (t // per_plane, row_of(*ids), t % per_plane)
        return pl.BlockSpec((1, rows, TILE), index)
    return [spec(off, n, per) for off, n, per in DZ_ARRAYS]


def _dh_norm1_bwd(dz, w_in_t, x, g1, dx1):
    tm = 1024

    def body(*refs):
        seg_refs = refs[:3]
        w_ref, x_ref, g_ref, dx1_ref, gx_ref, dg_ref, acc = refs[3:]
        i, kk = pl.program_id(0), pl.program_id(1)

        @pl.when(kk == 0)
        def _():
            acc[...] = jnp.zeros_like(acc)

        for s, (off, n, _) in enumerate(DZ_ARRAYS):
            @pl.when((kk >= off) & (kk < off + n))
            def _(s=s):
                acc[...] += _dot(seg_refs[s][0], w_ref[...])

        @pl.when((i == 0) & (kk == 0))
        def _():
            dg_ref[...] = jnp.zeros_like(dg_ref)

        @pl.when(kk == N_DZ_TILES - 1)
        def _():
            xv = x_ref[...]
            r = lax.rsqrt(jnp.mean(xv * xv, axis=-1, keepdims=True) + EPS)
            xh = xv * r
            dh = acc[...]
            dg_ref[...] += jnp.sum(dh * xh, axis=0, keepdims=True)
            gx_ref[...] = dx1_ref[...] + _rms_bwd(dh, xh, r, g_ref[...])

    tok = pl.BlockSpec((tm, D), lambda i, j: (i, 0))
    vec = pl.BlockSpec((1, D), lambda i, j: (0, 0))
    return pl.pallas_call(
        body, name="dh_norm1_bwd",
        out_shape=(jax.ShapeDtypeStruct((T, D), F32), jax.ShapeDtypeStruct((1, D), F32)),
        grid=(T // tm, N_DZ_TILES),
        in_specs=_dz_specs(tm, lambda i, j: j, lambda i, j: i)
        + [pl.BlockSpec((TILE, D), lambda i, j: (j, 0)), tok, vec, tok],
        out_specs=(tok, vec),
        scratch_shapes=[pltpu.VMEM((tm, D), F32)],
        compiler_params=_params(dimension_semantics=("arbitrary", "arbitrary")),
    )(*dz, w_in_t, x, g1, dx1)


def _grad_w_in(dz, h):
    def body(*refs):
        seg_refs = refs[:3]
        h_ref, gw_ref, gb_ref = refs[3:]
        j = pl.program_id(0)

        for s, (off, n, _) in enumerate(DZ_ARRAYS):
            @pl.when((j >= off) & (j < off + n))
            def _(s=s):
                a = seg_refs[s][0]
                gw_ref[...] = _dot_tn(a, h_ref[...]).astype(BF16)
                gb_ref[...] = jnp.sum(a.astype(F32), axis=0, keepdims=True)

    return pl.pallas_call(
        body, name="grad_w_in",
        out_shape=(jax.ShapeDtypeStruct((D_IN, D), BF16), jax.ShapeDtypeStruct((1, D_IN), F32)),
        grid=(N_DZ_TILES,),
        in_specs=_dz_specs(T, lambda j: j, lambda j: 0) + [pl.BlockSpec((T, D), lambda j: (0, 0))],
        out_specs=(pl.BlockSpec((TILE, D), lambda j: (j, 0)), pl.BlockSpec((1, TILE), lambda j: (0, j))),
        compiler_params=_params(dimension_semantics=("parallel",)),
    )(*dz, h)


def _rpb_rows(rpb):
    padded = jnp.pad(rpb, ((0, 0), (0, 0), (0, GRID_W - N_RPB_C)))
    rows = [padded[:, WIN_H - 1 - oi: 2 * WIN_H - 1 - oi].reshape(N_HEADS // 2, 2, KWIN)
            for oi in range(WIN_H)]
    return jnp.stack(rows, axis=0)


SKEW = KWIN - (WIN_W - 1)


MASKED = -1e30


def _bias_tiles(rows_ref, valid, bias_s):
    for oi in range(WIN_H):
        for hh in range(2):
            row = jnp.broadcast_to(rows_ref[oi, 0, hh:hh + 1, :], (GRID_W, KWIN))
            tile = pltpu.roll(row, SKEW, 1, stride=1, stride_axis=0)
            bias_s[oi, hh * GRID_W:(hh + 1) * GRID_W, :] = jnp.where(valid[:GRID_W], tile, MASKED)


def _bias_tile_grads(gb_s, flip, out_ref):
    for oi in range(WIN_H):
        for hh in range(2):
            g = _dot_exact(flip, gb_s[oi, hh * GRID_W:(hh + 1) * GRID_W, :])
            back = pltpu.roll(g, KWIN - (GRID_W - WIN_W), 1, stride=1, stride_axis=0)
            out_ref[0, oi, hh:hh + 1, :] = jnp.sum(back, axis=0, keepdims=True)


def _rpb_fold(row_grads):
    g = row_grads.transpose(1, 0, 2, 3).reshape(WIN_H, N_HEADS, WIN_H, GRID_W)
    g = g.transpose(0, 2, 1, 3)

    def body(g_ref, o_ref):
        for dr in range(N_RPB_R):
            terms = [g_ref[oi, i] for oi in range(WIN_H) for i in range(WIN_H) if i - oi + WIN_H - 1 == dr]
            acc = terms[0]
            for term in terms[1:]:
                acc = acc + term
            o_ref[dr] = acc

    out = pl.pallas_call(
        body, name="rpb_fold",
        out_shape=jax.ShapeDtypeStruct((N_RPB_R, N_HEADS, GRID_W), F32),
    )(g)
    return out.transpose(1, 0, 2)[:, :, :N_RPB_C]


def _att_scores(q_ref, k_ref, bias_ref, hmask, r):
    rs = jnp.clip(r - WIN_H // 2, 0, ROWS - WIN_H)
    oi = r - rs
    q0 = pl.multiple_of(r * GRID_W, GRID_W)
    k0 = pl.multiple_of(rs * GRID_W, GRID_W)
    q_r = q_ref[pl.ds(q0, GRID_W), :] * (DH ** -0.5)
    q2 = jnp.where(hmask, jnp.concatenate([q_r, q_r], axis=0), jnp.zeros((), BF16))
    kw = k_ref[pl.ds(k0, KWIN), :]
    s = _dot_nt(q2, kw) + bias_ref[oi]
    e = jnp.exp(s - jnp.max(s, axis=-1, keepdims=True))
    return e, 1.0 / jnp.sum(e, axis=-1, keepdims=True), q2, kw, q0, k0, oi


def _att_fwd(qkv, bias_rows):
    valid_np, hmask_np = _att_tables()

    def body(q_ref, k_ref, v_ref, rows_ref, valid_ref, hmask_ref, o_ref, bias_s):
        valid = valid_ref[...] > 0.5
        hmask = hmask_ref[...] > 0.5
        first_head = lax.broadcasted_iota(jnp.int32, (GRID_W, 2 * DH), 1) < DH
        _bias_tiles(rows_ref, valid, bias_s)

        def row(r, carry):
            e, rl, _, _, q0, k0, _ = _att_scores(q_ref, k_ref, bias_s, hmask, r)
            o2 = _dot((e * rl).astype(BF16), v_ref[pl.ds(k0, KWIN), :])
            o_ref[pl.ds(q0, GRID_W), :] = jnp.where(first_head, o2[:GRID_W], o2[GRID_W:]).astype(BF16)
            return carry

        lax.fori_loop(0, ROWS, row, 0, unroll=4)

    col = lambda off: pl.BlockSpec((T, 2 * DH), lambda hp: (0, hp + off))
    return pl.pallas_call(
        body, name="att_fwd",
        out_shape=jax.ShapeDtypeStruct((T, D_ATT), BF16),
        grid=(N_HEADS // 2,),
        in_specs=[col(0), col(4), col(8),
                  pl.BlockSpec((WIN_H, 1, 2, KWIN), lambda hp: (0, hp, 0, 0)),
                  pl.BlockSpec((2 * GRID_W, KWIN), lambda hp: (0, 0)),
                  pl.BlockSpec((2 * DH, 2 * DH), lambda hp: (0, 0))],
        out_specs=pl.BlockSpec((T, 2 * DH), lambda hp: (0, hp)),
        scratch_shapes=[pltpu.VMEM((WIN_H, 2 * GRID_W, KWIN), F32)],
        compiler_params=_params(dimension_semantics=("parallel",)),
    )(qkv, qkv, qkv, bias_rows, jnp.asarray(valid_np), jnp.asarray(hmask_np))


def _att_bwd(qkv, bias_rows, datt, after):
    valid_np, hmask_np = _att_tables()

    def body(q_ref, k_ref, v_ref, do_ref, rows_ref, valid_ref, hmask_ref, flip_ref,
             dqkv_ref, grows_ref, dk_acc, dv_acc, bias_s, gb_s):
        valid = valid_ref[...] > 0.5
        hmask = hmask_ref[...] > 0.5
        first_head = lax.broadcasted_iota(jnp.int32, (GRID_W, 2 * DH), 1) < DH
        dk_acc[...] = jnp.zeros_like(dk_acc)
        dv_acc[...] = jnp.zeros_like(dv_acc)
        gb_s[...] = jnp.zeros_like(gb_s)
        _bias_tiles(rows_ref, valid, bias_s)

        def row(r, carry):
            e, rl, q2, kw, q0, k0, oi = _att_scores(q_ref, k_ref, bias_s, hmask, r)
            do_r = do_ref[pl.ds(q0, GRID_W), :]
            do2 = jnp.where(hmask, jnp.concatenate([do_r, do_r], axis=0), jnp.zeros((), BF16))
            vw = v_ref[pl.ds(k0, KWIN), :]
            p = e * rl
            dp = _dot_nt(do2, vw)
            ds = p * (dp - jnp.sum(dp * p, axis=-1, keepdims=True))
            p16 = p.astype(BF16)
            ds16 = ds.astype(BF16)
            dv_acc[pl.ds(k0, KWIN), :] += _dot_tn(p16, do2)
            dk_acc[pl.ds(k0, KWIN), :] += _dot_tn(ds16, q2)
            dq2 = _dot(ds16, kw) * (DH ** -0.5)
            dqkv_ref[0, pl.ds(q0, GRID_W), :] = jnp.where(first_head, dq2[:GRID_W], dq2[GRID_W:]).astype(BF16)
            gb_s[oi] += ds
            return carry

        lax.fori_loop(0, ROWS, row, 0, unroll=4)
        dqkv_ref[1] = dk_acc[...].astype(BF16)
        dqkv_ref[2] = dv_acc[...].astype(BF16)
        _bias_tile_grads(gb_s, flip_ref[...], grows_ref)

    col = lambda off: pl.BlockSpec((T, 2 * DH), lambda hp: (0, hp + off))
    tiles = pltpu.VMEM((WIN_H, 2 * GRID_W, KWIN), F32)
    return pl.pallas_call(
        body, name="att_bwd",
        out_shape=(jax.ShapeDtypeStruct((3, T, D_ATT), BF16),
                   jax.ShapeDtypeStruct((N_HEADS // 2, WIN_H, 2, KWIN), F32)),
        grid=(N_HEADS // 2,),
        in_specs=[col(0), col(4), col(8), col(0),
                  pl.BlockSpec((WIN_H, 1, 2, KWIN), lambda hp: (0, hp, 0, 0)),
                  pl.BlockSpec((2 * GRID_W, KWIN), lambda hp: (0, 0)),
                  pl.BlockSpec((2 * DH, 2 * DH), lambda hp: (0, 0)),
                  pl.BlockSpec((GRID_W, GRID_W), lambda hp: (0, 0))],
        out_specs=(pl.BlockSpec((3, T, 2 * DH), lambda hp: (0, 0, hp)),
                   pl.BlockSpec((1, WIN_H, 2, KWIN), lambda hp: (hp, 0, 0, 0))),
        scratch_shapes=[pltpu.VMEM((T, 2 * DH), F32), pltpu.VMEM((T, 2 * DH), F32), tiles, tiles],
        compiler_params=_params(dimension_semantics=("parallel",)),
    )(qkv, qkv, qkv, datt, bias_rows, jnp.asarray(valid_np) + after, jnp.asarray(hmask_np),
      jnp.asarray(np.eye(GRID_W, dtype=np.float32)[::-1].copy()))


def _conv_taps(up):
    return (_shift_rows(up, 2), _shift_rows(up, 1), up, _shift_rows(up, -1))


def _pair_block_diag(w_pair, dup, same_half):
    return jnp.where(same_half, _dot(w_pair.astype(BF16), dup), 0.0).astype(BF16)


def _gates(u, u16, wa, ba, wi, bi, lam):
    r = _sigmoid(_dot(u16, wa) + ba)
    ig = _sigmoid(_dot(u16, wi) + bi)
    sp = _softplus(-lam)
    log_a = (-LRU_C) * r * sp
    a = jnp.exp(log_a)
    mult2 = jnp.maximum(_one_minus_square(log_a, a), 0.0)
    return r, ig, sp, a, jnp.sqrt(mult2), mult2


SCAN_BLOCKS = 2


def _scans(jobs):
    c = jobs[0][0].shape[1]
    nblk = T // 8
    rows = lax.broadcasted_iota(jnp.int32, (8, c), 0)

    def block(a, b, reverse):
        for s in (1, 2, 4):
            if reverse:
                keep = rows < 8 - s
                a_s = jnp.where(keep, pltpu.roll(a, 8 - s, 0), 1.0)
                b_s = jnp.where(keep, pltpu.roll(b, 8 - s, 0), 0.0)
            else:
                keep = rows >= s
                a_s = jnp.where(keep, pltpu.roll(a, s, 0), 1.0)
                b_s = jnp.where(keep, pltpu.roll(b, s, 0), 0.0)
            b = a * b_s + b
            a = a * a_s
        return a, b

    def step(i, carry):
        out = []
        for (a_ref, b_ref, h_ref, reverse), h_prev in zip(jobs, carry):
            for u in range(SCAN_BLOCKS):
                blk = i * SCAN_BLOCKS + u
                if reverse:
                    blk = nblk - 1 - blk
                t0 = pl.multiple_of(blk * 8, 8)
                a, b = block(a_ref[pl.ds(t0, 8), :], b_ref[pl.ds(t0, 8), :], reverse)
                h = a * h_prev + b
                h_ref[pl.ds(t0, 8), :] = h
                h_prev = jnp.broadcast_to(h[0:1] if reverse else h[7:8], (8, c))
            out.append(h_prev)
        return tuple(out)

    lax.fori_loop(0, nblk // SCAN_BLOCKS, step, tuple(jnp.zeros((8, c), F32) for _ in jobs))


def _rec_specs():
    tok = lambda off: pl.BlockSpec((T, CG), lambda g: (0, g + off))
    per_ch = lambda rows: pl.BlockSpec((rows, CG), lambda g: (0, g))
    wspec = pl.BlockSpec((2, 1, CG, REC_BLOCK), lambda g: (0, g, 0, 0))
    const = lambda shape: pl.BlockSpec(shape, lambda g: (0, 0))
    return tok, per_ch, wspec, const


def _rec_fwd(uy, conv_w, conv_b, w_a, b_a, w_i, b_i, lam):
    tok, per_ch, wspec, const = _rec_specs()

    def body(up_ref, yb_ref, cw_ref, cb_ref, wa_ref, ba_ref, wi_ref, bi_ref, lam_ref, dup_ref, half_ref,
             hf_ref, hb_ref, yrec_ref, a_f, bx_f, a_b, bx_b):
        dup = dup_ref[...]
        same_half = half_ref[...] > 0.5
        taps = _conv_taps(up_ref[...])
        u = cb_ref[...]
        for j in range(4):
            u = u + taps[j] * cw_ref[j:j + 1, :]
        u16 = u.astype(BF16)
        for d, (a_s, bx_s) in enumerate(((a_f, bx_f), (a_b, bx_b))):
            wa = _pair_block_diag(wa_ref[d, 0], dup, same_half)
            wi = _pair_block_diag(wi_ref[d, 0], dup, same_half)
            _, ig, _, a, mult, _ = _gates(u, u16, wa, ba_ref[d:d + 1, :], wi, bi_ref[d:d + 1, :],
                                       lam_ref[d:d + 1, :])
            a_s[...] = a
            bx_s[...] = mult * (ig * u)
        _scans([(a_f, bx_f, hf_ref, False), (a_b, bx_b, hb_ref, True)])
        gelu, _ = _gelu_and_grad(yb_ref[...])
        yrec_ref[...] = ((hf_ref[...] + hb_ref[...]) * gelu).astype(BF16)

    return pl.pallas_call(
        body, name="rec_fwd",
        out_shape=(jax.ShapeDtypeStruct((T, D_REC), F32), jax.ShapeDtypeStruct((T, D_REC), F32),
                   jax.ShapeDtypeStruct((T, D_REC), BF16)),
        grid=(N_CG,),
        in_specs=[tok(0), tok(N_CG), per_ch(4), per_ch(1), wspec, per_ch(2), wspec, per_ch(2), per_ch(2),
                  const((REC_BLOCK, CG)), const((CG, CG))],
        out_specs=(tok(0), tok(0), tok(0)),
        scratch_shapes=[pltpu.VMEM((T, CG), F32)] * 4,
        compiler_params=_params(dimension_semantics=("parallel",)),
    )(uy, uy, conv_w, conv_b, w_a, b_a, w_i, b_i, lam,
      jnp.asarray(_dup_table(), BF16), jnp.asarray(_pair_mask()))


def _rec_bwd(uy, hf, hb, dyrec, conv_w, conv_b, w_a, b_a, w_i, b_i, lam):
    tok, per_ch, wspec, const = _rec_specs()

    def body(up_ref, yb_ref, hf_ref, hb_ref, dy_ref, cw_ref, cb_ref, wa_ref, ba_ref, wi_ref, bi_ref,
             lam_ref, dup_ref, dupt_ref, half_ref,
             duy_ref, dcw_ref, dcb_ref, dwa_ref, dba_ref, dwi_ref, dbi_ref, dlam_ref,
             a_s0, a_s1, dh_s, g_s0, g_s1):
        dup = dup_ref[...]
        dup_t = dupt_ref[...]
        same_half = half_ref[...] > 0.5
        taps = _conv_taps(up_ref[...])
        u = cb_ref[...]
        for j in range(4):
            u = u + taps[j] * cw_ref[j:j + 1, :]
        u16 = u.astype(BF16)
        gelu, dgelu = _gelu_and_grad(yb_ref[...])
        dy = dy_ref[...]
        duy_ref[1] = (dy * (hf_ref[...] + hb_ref[...]) * dgelu).astype(BF16)
        dh_s[...] = dy * gelu
        gate_values = []
        for d, a_s in enumerate((a_s0, a_s1)):
            wa = _pair_block_diag(wa_ref[d, 0], dup, same_half)
            wi = _pair_block_diag(wi_ref[d, 0], dup, same_half)
            lam_d = lam_ref[d:d + 1, :]
            r, ig, sp, a, mult, mult2 = _gates(u, u16, wa, ba_ref[d:d + 1, :], wi, bi_ref[d:d + 1, :], lam_d)
            a_s[...] = _shift_rows(a, 1 if d == 1 else -1)
            gate_values.append((wa, wi, lam_d, r, ig, sp, a, mult, mult2))
        _scans([(a_s0, dh_s, g_s0, True), (a_s1, dh_s, g_s1, False)])
        du = jnp.zeros((T, CG), F32)
        for d, g_s in enumerate((g_s0, g_s1)):
            reverse = d == 1
            wa, wi, lam_d, r, ig, sp, a, mult, mult2 = gate_values[d]
            g = g_s[...]
            h_prev = _shift_rows(hb_ref[...], -1) if reverse else _shift_rows(hf_ref[...], 1)
            da = g * h_prev
            dmult = g * (ig * u)
            dig = g * mult * u
            du = du + g * mult * ig
            dmult_dlog = jnp.where(mult2 > 0.0, -(a * a) * lax.rsqrt(mult2), 0.0)
            dlog_a = da * a + dmult * dmult_dlog
            dr = dlog_a * ((-LRU_C) * sp)
            dsp = jnp.sum(dlog_a * ((-LRU_C) * r), axis=0, keepdims=True)
            dlam_ref[d:d + 1, :] = dsp * (-_sigmoid(-lam_d))
            dga = dr * r * (1.0 - r)
            dgi = dig * ig * (1.0 - ig)
            dga16 = dga.astype(BF16)
            dgi16 = dgi.astype(BF16)
            du = du + _dot_nt(dga16, wa) + _dot_nt(dgi16, wi)
            dwa_ref[d, 0] = _dot_exact(jnp.where(same_half, _dot_tn(u16, dga16), 0.0), dup_t)
            dwi_ref[d, 0] = _dot_exact(jnp.where(same_half, _dot_tn(u16, dgi16), 0.0), dup_t)
            dba_ref[d:d + 1, :] = jnp.sum(dga, axis=0, keepdims=True)
            dbi_ref[d:d + 1, :] = jnp.sum(dgi, axis=0, keepdims=True)
        dcb_ref[...] = jnp.sum(du, axis=0, keepdims=True)
        for j in range(4):
            dcw_ref[j:j + 1, :] = jnp.sum(du * taps[j], axis=0, keepdims=True)
        dup_in = (_shift_rows(du, -2) * cw_ref[0:1, :] + _shift_rows(du, -1) * cw_ref[1:2, :]
                  + du * cw_ref[2:3, :] + _shift_rows(du, 1) * cw_ref[3:4, :])
        duy_ref[0] = dup_in.astype(BF16)

    wshape = jax.ShapeDtypeStruct((2, N_CG, CG, REC_BLOCK), F32)
    vec = lambda rows: jax.ShapeDtypeStruct((rows, D_REC), F32)
    dup_np = _dup_table()
    return pl.pallas_call(
        body, name="rec_bwd",
        out_shape=(jax.ShapeDtypeStruct((2, T, D_REC), BF16),
                   vec(4), vec(1), wshape, vec(2), wshape, vec(2), vec(2)),
        grid=(N_CG,),
        in_specs=[tok(0), tok(N_CG), tok(0), tok(0), tok(0),
                  per_ch(4), per_ch(1), wspec, per_ch(2), wspec, per_ch(2), per_ch(2),
                  const((REC_BLOCK, CG)), const((CG, REC_BLOCK)), const((CG, CG))],
        out_specs=(pl.BlockSpec((2, T, CG), lambda g: (0, 0, g)),
                   per_ch(4), per_ch(1), wspec, per_ch(2), wspec, per_ch(2), per_ch(2)),
        scratch_shapes=[pltpu.VMEM((T, CG), F32)] * 5,
        compiler_params=_params(dimension_semantics=("parallel",)),
    )(uy, uy, hf, hb, dyrec, conv_w, conv_b, w_a, b_a, w_i, b_i, lam,
      jnp.asarray(dup_np, BF16), jnp.asarray(dup_np.T.copy()), jnp.asarray(_pair_mask()))


TM_MIX = 256


def _mix_specs():
    tok = lambda width, blk=0: pl.BlockSpec((TM_MIX, width), lambda i: (i, blk))
    full = lambda shape: pl.BlockSpec(shape, lambda i: (0, 0))
    return tok, full


def _mix_fwd(x, att, yrec, gg, w_att_o_t, w_rec_o, w_out):
    tok, full = _mix_specs()

    def body(x_ref, att_ref, yr_ref, ga_ref, gr_ref, wao_ref, wro_ref, wo_ref, x1_ref, mixed_ref):
        y_att = _dot_nt(att_ref[...], wao_ref[...])
        y_rec = _dot(yr_ref[...], wro_ref[...])
        mixed = (_sigmoid(ga_ref[...]) * y_att + _sigmoid(gr_ref[...]) * y_rec).astype(BF16)
        mixed_ref[...] = mixed
        x1_ref[...] = x_ref[...] + _dot(mixed, wo_ref[...])

    return pl.pallas_call(
        body, name="mix_fwd",
        out_shape=(jax.ShapeDtypeStruct((T, D), F32), jax.ShapeDtypeStruct((T, D), BF16)),
        grid=(T // TM_MIX,),
        in_specs=[tok(D), tok(D_ATT), tok(D_REC), tok(D, 0), tok(D, 1),
                  full((D, D_ATT)), full((D_REC, D)), full((D, D))],
        out_specs=(tok(D), tok(D)),
        compiler_params=_params(dimension_semantics=("parallel",)),
    )(x, att, yrec, gg, gg, w_att_o_t, w_rec_o, w_out)


def _mix_bwd(dx1, att, yrec, gg, w_att_o_t, w_rec_o, w_out):
    tok, full = _mix_specs()

    def body(dx_ref, att_ref, yr_ref, ga_ref, gr_ref, wao_ref, wro_ref, wo_ref,
             dgg_ref, dya_ref, dyr_ref, datt_ref, dyrp_ref):
        dmixed = _dot_nt(dx_ref[...].astype(BF16), wo_ref[...])
        y_att = _dot_nt(att_ref[...], wao_ref[...])
        y_rec = _dot(yr_ref[...], wro_ref[...])
        sa = _sigmoid(ga_ref[...])
        sr = _sigmoid(gr_ref[...])
        dgg_ref[0] = (dmixed * y_att * sa * (1.0 - sa)).astype(BF16)
        dgg_ref[1] = (dmixed * y_rec * sr * (1.0 - sr)).astype(BF16)
        dya = (dmixed * sa).astype(BF16)
        dyr = (dmixed * sr).astype(BF16)
        dya_ref[...] = dya
        dyr_ref[...] = dyr
        datt_ref[...] = _dot(dya, wao_ref[...]).astype(BF16)
        dyrp_ref[...] = _dot_nt(dyr, wro_ref[...])

    return pl.pallas_call(
        body, name="mix_bwd",
        out_shape=(jax.ShapeDtypeStruct((2, T, D), BF16),
                   jax.ShapeDtypeStruct((T, D), BF16), jax.ShapeDtypeStruct((T, D), BF16),
                   jax.ShapeDtypeStruct((T, D_ATT), BF16), jax.ShapeDtypeStruct((T, D_REC), F32)),
        grid=(T // TM_MIX,),
        in_specs=[tok(D), tok(D_ATT), tok(D_REC), tok(D, 0), tok(D, 1),
                  full((D, D_ATT)), full((D_REC, D)), full((D, D))],
        out_specs=(pl.BlockSpec((2, TM_MIX, D), lambda i: (0, i, 0)),
                   tok(D), tok(D), tok(D_ATT), tok(D_REC)),
        compiler_params=_params(dimension_semantics=("parallel",)),
    )(dx1, att, yrec, gg, gg, w_att_o_t, w_rec_o, w_out)


TM_FFN = 256
FF_CHUNK = 1024


def _ffn_loss(x1, target, g2, gf, w_ff1_t, w_ff2):
    n_chunks = D_FF // FF_CHUNK

    def body(x1_ref, tg_ref, g2_ref, gf_ref, w1_hbm, w2_hbm,
             loss_ref, dx1_ref, h2_ref, act_ref, dpre_ref, dx2_ref, dg2_ref, dgf_ref,
             w1, w2, relu_s):
        i = pl.program_id(0)

        @pl.when(i == 0)
        def _():
            pltpu.sync_copy(w1_hbm, w1)
            pltpu.sync_copy(w2_hbm, w2)
            loss_ref[...] = jnp.zeros_like(loss_ref)
            dg2_ref[...] = jnp.zeros_like(dg2_ref)
            dgf_ref[...] = jnp.zeros_like(dgf_ref)

        x1v = x1_ref[...]
        r2 = lax.rsqrt(jnp.mean(x1v * x1v, axis=-1, keepdims=True) + EPS)
        xh2 = x1v * r2
        h2 = (xh2 * g2_ref[...]).astype(BF16)
        h2_ref[...] = h2
        x2 = x1v
        for c in range(n_chunks):
            ff = slice(c * FF_CHUNK, (c + 1) * FF_CHUNK)
            rl = jnp.maximum(_dot_nt(h2, w1[ff, :]), 0.0)
            relu_s[:, ff] = rl
            act = (rl * rl).astype(BF16)
            act_ref[:, ff] = act
            x2 = x2 + _dot(act, w2[ff, :])
        r3 = lax.rsqrt(jnp.mean(x2 * x2, axis=-1, keepdims=True) + EPS)
        xh3 = x2 * r3
        err = xh3 * gf_ref[...] - tg_ref[...]
        loss_ref[...] += 0.5 * jnp.sum(jnp.mean(err * err, axis=-1, keepdims=True))
        dy = err * (1.0 / D)
        dgf_ref[...] += jnp.sum(dy * xh3, axis=0, keepdims=True)
        dx2 = _rms_bwd(dy, xh3, r3, gf_ref[...])
        dx2_16 = dx2.astype(BF16)
        dx2_ref[...] = dx2_16
        dh2 = jnp.zeros((TM_FFN, D), F32)
        for c in range(n_chunks):
            ff = slice(c * FF_CHUNK, (c + 1) * FF_CHUNK)
            dpre = (_dot_nt(dx2_16, w2[ff, :]) * (2.0 * relu_s[:, ff])).astype(BF16)
            dpre_ref[:, ff] = dpre
            dh2 = dh2 + _dot(dpre, w1[ff, :])
        dg2_ref[...] += jnp.sum(dh2 * xh2, axis=0, keepdims=True)
        dx1_ref[...] = dx2 + _rms_bwd(dh2, xh2, r2, g2_ref[...])

    tok = lambda width: pl.BlockSpec((TM_FFN, width), lambda i: (i, 0))
    vec = pl.BlockSpec((1, D), lambda i: (0, 0))
    hbm = pl.BlockSpec(memory_space=pl.ANY)
    return pl.pallas_call(
        body, name="ffn_loss",
        out_shape=(jax.ShapeDtypeStruct((8, 128), F32), jax.ShapeDtypeStruct((T, D), F32),
                   jax.ShapeDtypeStruct((T, D), BF16), jax.ShapeDtypeStruct((T, D_FF), BF16),
                   jax.ShapeDtypeStruct((T, D_FF), BF16), jax.ShapeDtypeStruct((T, D), BF16),
                   jax.ShapeDtypeStruct((1, D), F32), jax.ShapeDtypeStruct((1, D), F32)),
        grid=(T // TM_FFN,),
        in_specs=[tok(D), tok(D), vec, vec, hbm, hbm],
        out_specs=(pl.BlockSpec((8, 128), lambda i: (0, 0)), tok(D), tok(D), tok(D_FF), tok(D_FF), tok(D),
                   vec, vec),
        scratch_shapes=[pltpu.VMEM((D_FF, D), BF16), pltpu.VMEM((D_FF, D), BF16),
                        pltpu.VMEM((TM_FFN, D_FF), F32)],
        compiler_params=_params(dimension_semantics=("arbitrary",)),
    )(x1, target, g2, gf, w_ff1_t, w_ff2)


def _local_step(x, target, p, late_weights, reduce_early):
    bias = _rpb_rows(p["rpb"])
    pairs = lambda w: w.reshape(2, N_CG, CG, REC_BLOCK)
    w_a, w_i = pairs(p["w_rg_a"]), pairs(p["w_rg_i"])
    rec_params = (p["conv_w"], p["conv_b"], w_a, p["b_rg_a"], w_i, p["b_rg_i"], p["lru_lambda"])

    qkv, uy, gg, h = _in_proj(x, p["ln1_g"], p["w_in_t"], p["b_in"])
    att = _att_fwd(qkv, bias)
    hf, hb, yrec = _rec_fwd(uy, *rec_params)
    p = {**p, **late_weights(yrec)}
    x1, mixed = _mix_fwd(x, att, yrec, gg, p["w_att_o_t"], p["w_rec_o"], p["w_out"])
    loss8, dx1, h2, act, dpre, dx2, g_ln2, g_lnf = _ffn_loss(
        x1, target, p["ln2_g"], p["lnf_g"], p["w_ff1_t"], p["w_ff2"])

    dgg, dya, dyr, datt, dyrp = _mix_bwd(dx1, att, yrec, gg, p["w_att_o_t"], p["w_rec_o"], p["w_out"])
    duy, g_cw, g_cb, g_wa, g_ba, g_wi, g_bi, g_lam = _rec_bwd(uy, hf, hb, dyrp, *rec_params)
    blocks = lambda g: g.reshape(2, N_REC_BLOCKS, REC_BLOCK, REC_BLOCK)
    grads = {
        "w_att_o_t": _matmul(dya, att, "tn", BF16, "g_w_att_o"),
        "conv_w": g_cw, "conv_b": g_cb, "w_rg_a": blocks(g_wa), "b_rg_a": g_ba,
        "w_rg_i": blocks(g_wi), "b_rg_i": g_bi, "lru_lambda": g_lam,
        "w_rec_o": _matmul(yrec, dyr, "tn", BF16, "g_w_rec_o"),
        "w_out": _matmul(mixed, dx1, "tn", BF16, "g_w_out"),
        "ln2_g": g_ln2,
        "w_ff1_t": _matmul(dpre, h2, "tn", BF16, "g_w_ff1"),
        "w_ff2": _matmul(act, dx2, "tn", BF16, "g_w_ff2"),
        "lnf_g": g_lnf,
    }
    dqkv, gbias = _att_bwd(qkv, bias, datt, reduce_early(grads))
    dz = (dqkv, duy, dgg)
    grad_x, g_ln1 = _dh_norm1_bwd(dz, p["w_in_t"], x, p["ln1_g"], dx1)
    g_w_in_t, g_b_in = _grad_w_in(dz, h)
    grads.update(ln1_g=g_ln1, w_in_t=g_w_in_t, b_in=g_b_in, rpb=_rpb_fold(gbias))
    return loss8[0:1, 0:1], grad_x, grads


MESH_ID = pl.DeviceIdType.MESH
ANY = pl.BlockSpec(memory_space=pl.ANY)

CHAN_BLOCK_ROWS = 32
GATE_ROWS = 2 * 2 * N_REC_BLOCKS * REC_BLOCK * REC_BLOCK // (N_DEV * D)
SECTIONS = (("w_in_t", 704, D), ("w_rec_o", 128, D), ("w_out", 128, D), ("w_ff1_t", 512, D),
            ("w_ff2", 512, D), ("chan", CHAN_BLOCK_ROWS, D), ("w_att_o_t", 128, D_ATT),
            ("gates", GATE_ROWS, D))
N_SEC = len(SECTIONS)
N_CHAN_ROWS = 10
CHAN = (("conv_w", 4), ("b_rg_a", 2), ("b_rg_i", 2), ("lru_lambda", 2))


def _position():
    return lax.axis_index("x"), lax.axis_index("y"), lax.axis_index("c")


def _other_chips(x, y):
    return [(1 - x, y), (x, 1 - y), (1 - x, 1 - y)]


PASS_ON_ID, PAIR_EARLY_ID, PAIR_LATE_ID = 1, 2, 3


def _pair_handshake(x, y, c):
    barrier = pltpu.get_barrier_semaphore()
    pl.semaphore_signal(barrier, inc=1, device_id=(x, y, 1 - c), device_id_type=MESH_ID)
    pl.semaphore_wait(barrier, 1)


def _block_of(ref, dev, rows):
    return ref.at[pl.ds(pl.multiple_of(dev * rows, 16), rows)]


def _all_gather(shards, name):
    ns = len(shards)

    def body(*refs):
        x_refs, out_refs, done_ref = refs[:ns], refs[ns:2 * ns], refs[2 * ns]
        send_sems, recv_sems, local_sems = refs[2 * ns + 1:]
        done_ref[0, 0] = 0.0
        x, y, c = _position()
        me, sibling = (x, y, c), (x, y, 1 - c)
        x_nbr, y_nbr, diagonal = _other_chips(x, y)
        north = c == 1
        relay_from = (jnp.where(north, x_nbr[0], y_nbr[0]), jnp.where(north, x_nbr[1], y_nbr[1]))
        relay_to = (jnp.where(north, y_nbr[0], x_nbr[0]), jnp.where(north, y_nbr[1], x_nbr[1]))

        def rows(s, px, py, pc):
            return _block_of(out_refs[s], 4 * px + 2 * py + pc, shards[s].shape[0])

        def copy(k, s, block, to, from_shard=False):
            return pltpu.make_async_remote_copy(
                src_ref=x_refs[s] if from_shard else rows(s, *block), dst_ref=rows(s, *block),
                send_sem=send_sems.at[k * ns + s], recv_sem=recv_sems.at[k * ns + s],
                device_id=to, device_id_type=MESH_ID)

        sections = range(ns)
        mine = [pltpu.make_async_copy(x_refs[s], rows(s, *me), local_sems.at[s]) for s in sections]
        sent = [copy(k, s, me, to, True) for k, to in enumerate((sibling, (*x_nbr, c), (*y_nbr, c)))
                for s in sections]
        for cp in mine + sent:
            cp.start()
        for s in sections:
            copy(1, s, (*x_nbr, c), me).wait_recv()
            copy(2, s, (*y_nbr, c), me).wait_recv()
            sent += [copy(3, s, (*relay_from, c), (*relay_to, c)),
                     copy(4, s, (*x_nbr, c), sibling), copy(5, s, (*y_nbr, c), sibling)]
            for cp in sent[-3:]:
                cp.start()
        for s in sections:
            copy(3, s, (*diagonal, c), me).wait_recv()
            sent.append(copy(6, s, (*diagonal, c), sibling))
            sent[-1].start()
        for s in sections:
            copy(0, s, sibling, me).wait_recv()
            for k, chip in ((4, x_nbr), (5, y_nbr), (6, diagonal)):
                copy(k, s, (*chip, 1 - c), me).wait_recv()
        for cp in sent:
            cp.wait_send()
        for cp in mine:
            cp.wait()

    return pl.pallas_call(
        body, name=name,
        out_shape=tuple(jax.ShapeDtypeStruct((N_DEV * s.shape[0], s.shape[1]), s.dtype) for s in shards)
        + (jax.ShapeDtypeStruct((1, 1), F32),),
        in_specs=[ANY] * ns,
        out_specs=(ANY,) * ns + (pl.BlockSpec(memory_space=pltpu.SMEM),),
        scratch_shapes=[pltpu.SemaphoreType.DMA((7 * ns,)), pltpu.SemaphoreType.DMA((7 * ns,)),
                        pltpu.SemaphoreType.DMA((ns,))],
    )(*shards)


HBM = pl.BlockSpec(memory_space=pltpu.HBM)
SEM = pl.BlockSpec(memory_space=pltpu.SEMAPHORE)
EFFECT = pltpu.SideEffectType.DATAFLOW_SIDE_EFFECTING


def _in_hbm(a):
    return pltpu.with_memory_space_constraint(a, pltpu.HBM)


def _first_hop_copies(shards, x_refs, zones, send_sems, recv_sems):
    ns = len(shards)
    x, y, c = _position()
    targets = [(x, y, 1 - c)] + [(cx, cy, c) for cx, cy in _other_chips(x, y)]
    return [pltpu.make_async_remote_copy(
        src_ref=x_refs[s], dst_ref=_block_of(zones[s], 4 * x + 2 * y + c, shards[s].shape[0]),
        send_sem=send_sems.at[k * ns + s], recv_sem=recv_sems.at[k * ns + s],
        device_id=to, device_id_type=MESH_ID)
        for k, to in enumerate(targets) for s in range(ns)]


def _after_all(arrays, name):
    def body(*refs):
        refs[-1][...] = jnp.zeros_like(refs[-1])

    return pl.pallas_call(
        body, name=name,
        out_shape=jax.ShapeDtypeStruct((8, LANES), F32),
        in_specs=[pl.BlockSpec(memory_space=pl.ANY)] * len(arrays),
        out_specs=pl.BlockSpec(memory_space=pltpu.VMEM),
    )(*arrays)


def _own_blocks_placed(shards, after):
    ns = len(shards)
    x, y, c = _position()
    me = jnp.reshape(4 * x + 2 * y + c, (1,)).astype(jnp.int32)
    shards = [*shards[:-1], shards[-1] + after.astype(shards[-1].dtype)]

    def body(me_ref, *refs):
        for s in range(ns):
            refs[ns + s][...] = refs[s][...]

    return pl.pallas_call(
        body, name="own_blocks_placed",
        out_shape=tuple(jax.ShapeDtypeStruct((N_DEV * s.shape[0], s.shape[1]), s.dtype) for s in shards),
        grid_spec=pltpu.PrefetchScalarGridSpec(
            num_scalar_prefetch=1, grid=(1,),
            in_specs=[pl.BlockSpec(s.shape, lambda i, me: (0, 0)) for s in shards],
            out_specs=tuple(pl.BlockSpec(s.shape, lambda i, me: (me[0], 0)) for s in shards)),
        compiler_params=_params(dimension_semantics=("arbitrary",)),
    )(me, *shards)


def _gather_start(shards, after, name):
    ns = len(shards)
    zones = _own_blocks_placed(shards, after)

    def body(*refs):
        for cp in _first_hop_copies(shards, refs[:ns], refs[ns:2 * ns], refs[2 * ns], refs[2 * ns + 1]):
            cp.start()
        refs[-1][...] = jnp.zeros_like(refs[-1])

    out = pl.pallas_call(
        body, name=name,
        out_shape=(pltpu.SemaphoreType.DMA((4 * ns,)), pltpu.SemaphoreType.DMA((4 * ns,)),
                   *[pltpu.HBM(a.shape, a.dtype) for a in (*shards, *zones)],
                   jax.ShapeDtypeStruct((8, LANES), F32)),
        in_specs=[HBM] * (2 * ns),
        out_specs=(SEM, SEM, *[HBM] * (2 * ns), pl.BlockSpec(memory_space=pltpu.VMEM)),
        input_output_aliases={i: 2 + i for i in range(2 * ns)},
        compiler_params=pltpu.CompilerParams(has_side_effects=EFFECT),
    )(*[_in_hbm(a) for a in shards], *[_in_hbm(a) for a in zones])
    return out[0], out[1], out[2:2 + ns], out[2 + ns:2 + 2 * ns], out[-1]


def _gather_wait(send_sems, recv_sems, shards, zones, after, name):
    ns = len(shards)

    def body(*refs):
        for cp in _first_hop_copies(shards, refs[:ns], refs[ns:2 * ns], refs[2 * ns], refs[2 * ns + 1]):
            cp.wait_send()
            cp.wait_recv()

    out = pl.pallas_call(
        body, name=name,
        out_shape=tuple(pltpu.HBM(a.shape, a.dtype) for a in (*shards, *zones)),
        in_specs=[HBM] * (2 * ns) + [SEM, SEM, ANY],
        out_specs=(HBM,) * (2 * ns),
        input_output_aliases={i: i for i in range(2 * ns)},
        compiler_params=pltpu.CompilerParams(has_side_effects=EFFECT),
    )(*shards, *zones, send_sems, recv_sems, after)
    return out[ns:]


def _gather_pass_on(rows, zones, name):
    ns = len(zones)

    def body(*refs):
        in_refs, out_refs = refs[:ns], refs[ns:2 * ns]
        send_sems, recv_sems = refs[2 * ns:]
        x, y, c = _position()
        _pair_handshake(x, y, c)
        copies = [pltpu.make_async_remote_copy(
            src_ref=_block_of(in_refs[s], 4 * cx + 2 * cy + c, rows[s]),
            dst_ref=_block_of(out_refs[s], 4 * cx + 2 * cy + c, rows[s]),
            send_sem=send_sems.at[j * ns + s], recv_sem=recv_sems.at[j * ns + s],
            device_id=(x, y, 1 - c), device_id_type=MESH_ID)
            for j, (cx, cy) in enumerate(_other_chips(x, y)) for s in range(ns)]
        for cp in copies:
            cp.start()
        for cp in copies:
            cp.wait_recv()
        for cp in copies:
            cp.wait_send()

    return pl.pallas_call(
        body, name=name,
        out_shape=tuple(jax.ShapeDtypeStruct(z.shape, z.dtype) for z in zones),
        in_specs=[ANY] * ns, out_specs=(ANY,) * ns,
        input_output_aliases={i: i for i in range(ns)},
        scratch_shapes=[pltpu.SemaphoreType.DMA((3 * ns,)), pltpu.SemaphoreType.DMA((3 * ns,))],
        compiler_params=pltpu.CompilerParams(collective_id=PASS_ON_ID),
    )(*zones)


def _pair_copies(sections, g_refs, land, send_sems, recv_sems):
    ns = len(sections)
    x, y, c = _position()
    return [pltpu.make_async_remote_copy(
        src_ref=_block_of(g_refs[s], 2 * k + 1 - c, rows), dst_ref=land[s].at[k],
        send_sem=send_sems.at[k * ns + s], recv_sem=recv_sems.at[k * ns + s],
        device_id=(x, y, 1 - c), device_id_type=MESH_ID)
        for k in range(N_CHIPS) for s, (_, rows, _) in enumerate(sections)]


def _pair_exchange_start(sections, grads, barrier_id, name):
    ns = len(sections)

    def body(*refs):
        _pair_handshake(*_position())
        for cp in _pair_copies(sections, refs[:ns], refs[ns:2 * ns], refs[2 * ns], refs[2 * ns + 1]):
            cp.start()
        refs[-1][...] = jnp.zeros_like(refs[-1])

    zones = [lax.empty((N_CHIPS, rows, cols), BF16) for _, rows, cols in sections]
    n = N_CHIPS * ns
    out = pl.pallas_call(
        body, name=name,
        out_shape=(pltpu.SemaphoreType.DMA((n,)), pltpu.SemaphoreType.DMA((n,)),
                   *[pltpu.HBM(a.shape, a.dtype) for a in (*grads, *zones)],
                   jax.ShapeDtypeStruct((8, LANES), F32)),
        in_specs=[HBM] * (2 * ns),
        out_specs=(SEM, SEM, *[HBM] * (2 * ns), pl.BlockSpec(memory_space=pltpu.VMEM)),
        input_output_aliases={i: 2 + i for i in range(2 * ns)},
        compiler_params=pltpu.CompilerParams(has_side_effects=EFFECT, collective_id=barrier_id),
    )(*[_in_hbm(a) for a in grads], *[_in_hbm(a) for a in zones])
    return out[0], out[1], out[2:2 + ns], out[2 + ns:2 + 2 * ns], out[-1]


def _pair_exchange_wait(sections, send_sems, recv_sems, grads, zones, after, name):
    ns = len(sections)

    def body(*refs):
        for cp in _pair_copies(sections, refs[:ns], refs[ns:2 * ns], refs[2 * ns], refs[2 * ns + 1]):
            cp.wait_send()
            cp.wait_recv()

    out = pl.pallas_call(
        body, name=name,
        out_shape=tuple(pltpu.HBM(a.shape, a.dtype) for a in (*grads, *zones)),
        in_specs=[HBM] * (2 * ns) + [SEM, SEM, ANY],
        out_specs=(HBM,) * (2 * ns),
        input_output_aliases={i: i for i in range(2 * ns)},
        compiler_params=pltpu.CompilerParams(has_side_effects=EFFECT),
    )(*grads, *zones, send_sems, recv_sems, after)
    return out[:ns], out[ns:]


def _pair_add(sections, grads, got, core, name):
    ns = len(sections)

    def body(core_ref, *refs):
        g_refs, got_refs, p_refs = refs[:ns], refs[ns:2 * ns], refs[2 * ns:]
        for s in range(ns):
            p_refs[s][0] = (g_refs[s][...].astype(F32) + got_refs[s][0].astype(F32)).astype(BF16)

    slot = [pl.BlockSpec((1, rows, cols), lambda k, c: (k, 0, 0)) for _, rows, cols in sections]
    return pl.pallas_call(
        body, name=name,
        out_shape=tuple(jax.ShapeDtypeStruct((N_CHIPS, rows, cols), BF16) for _, rows, cols in sections),
        grid_spec=pltpu.PrefetchScalarGridSpec(
            num_scalar_prefetch=1, grid=(N_CHIPS,),
            in_specs=[pl.BlockSpec((rows, cols), lambda k, c: (2 * k + c[0], 0)) for _, rows, cols in sections]
            + slot,
            out_specs=tuple(slot)),
        compiler_params=_params(dimension_semantics=("parallel",)),
    )(core, *grads, *got)


def _chip_copies(sections, p_refs, land, send_sems, recv_sems):
    ns = len(sections)
    x, y, c = _position()
    return [pltpu.make_async_remote_copy(
        src_ref=p_refs[s].at[2 * cx + cy], dst_ref=land[s].at[j],
        send_sem=send_sems.at[j * ns + s], recv_sem=recv_sems.at[j * ns + s],
        device_id=(cx, cy, c), device_id_type=MESH_ID)
        for j, (cx, cy) in enumerate(_other_chips(x, y)) for s in range(ns)]


def _chip_exchange(sections, parts, name):
    ns = len(sections)

    def body(*refs):
        copies = _chip_copies(sections, refs[:ns], refs[ns:2 * ns], *refs[2 * ns:])
        for cp in copies:
            cp.start()
        for cp in copies:
            cp.wait_recv()
        for cp in copies:
            cp.wait_send()

    n = 3 * ns
    return pl.pallas_call(
        body, name=name,
        out_shape=tuple(jax.ShapeDtypeStruct((3, rows, cols), BF16) for _, rows, cols in sections),
        in_specs=[ANY] * ns, out_specs=(ANY,) * ns,
        scratch_shapes=[pltpu.SemaphoreType.DMA((n,)), pltpu.SemaphoreType.DMA((n,))],
    )(*parts)


def _chip_exchange_start(sections, parts, name):
    ns = len(sections)

    def body(*refs):
        p_refs, land = refs[:ns], refs[ns:2 * ns]
        send_sems, recv_sems = refs[2 * ns], refs[2 * ns + 1]
        token = refs[-1]
        for cp in _chip_copies(sections, p_refs, land, send_sems, recv_sems):
            cp.start()
        token[...] = jnp.zeros_like(token)

    zones = [lax.empty((3, rows, cols), BF16) for _, rows, cols in sections]
    out = pl.pallas_call(
        body, name=name,
        out_shape=(pltpu.SemaphoreType.DMA((3 * ns,)), pltpu.SemaphoreType.DMA((3 * ns,)),
                   *[pltpu.HBM(a.shape, a.dtype) for a in parts], *[pltpu.HBM(a.shape, a.dtype) for a in zones],
                   jax.ShapeDtypeStruct((8, LANES), F32)),
        in_specs=[HBM] * (2 * ns),
        out_specs=(SEM, SEM, *[HBM] * (2 * ns), pl.BlockSpec(memory_space=pltpu.VMEM)),
        input_output_aliases={i: 2 + i for i in range(2 * ns)},
        compiler_params=pltpu.CompilerParams(has_side_effects=EFFECT),
    )(*[_in_hbm(a) for a in parts], *[_in_hbm(a) for a in zones])
    return out[0], out[1], out[2:2 + ns], out[2 + ns:2 + 2 * ns], out[-1]


def _chip_exchange_wait(sections, send_sems, recv_sems, parts, zones, after, name):
    ns = len(sections)

    def body(*refs):
        p_refs, land = refs[:ns], refs[ns:2 * ns]
        for cp in _chip_copies(sections, p_refs, land, refs[2 * ns], refs[2 * ns + 1]):
            cp.wait_send()
            cp.wait_recv()

    out = pl.pallas_call(
        body, name=name,
        out_shape=tuple(pltpu.HBM(a.shape, a.dtype) for a in (*parts, *zones)),
        in_specs=[HBM] * (2 * ns) + [SEM, SEM, ANY],
        out_specs=(HBM,) * (2 * ns),
        input_output_aliases={i: i for i in range(2 * ns)},
        compiler_params=pltpu.CompilerParams(has_side_effects=EFFECT),
    )(*parts, *zones, send_sems, recv_sems, after)
    return out[:ns], out[ns:]


def _grad_finish(sections, parts, far, chip, name):
    ns = len(sections)

    def body(chip_ref, *refs):
        p_refs, b_refs, g_refs = refs[:ns], refs[ns:2 * ns], refs[2 * ns:]
        for s in range(ns):
            g = p_refs[s][0].astype(F32)
            for j in range(3):
                g = g + b_refs[s][j].astype(F32)
            g_refs[s][...] = g

    half = [(rows // 2, cols) for _, rows, cols in sections]
    return pl.pallas_call(
        body, name=name,
        out_shape=tuple(jax.ShapeDtypeStruct((rows, cols), F32) for _, rows, cols in sections),
        grid_spec=pltpu.PrefetchScalarGridSpec(
            num_scalar_prefetch=1, grid=(2,),
            in_specs=[pl.BlockSpec((1, r, c), lambda i, chip: (chip[0], i, 0)) for r, c in half]
            + [pl.BlockSpec((3, r, c), lambda i, chip: (0, i, 0)) for r, c in half],
            out_specs=tuple(pl.BlockSpec((r, c), lambda i, chip: (i, 0)) for r, c in half)),
        compiler_params=_params(dimension_semantics=("parallel",)),
    )(chip, *parts, *far)


def _sum_devices(parts, rows, name):
    cols = parts.shape[1]
    tr = rows // 2

    def body(*refs):
        s = refs[0][...].astype(F32)
        for d in range(1, N_DEV):
            s = s + refs[d][...].astype(F32)
        refs[N_DEV][...] = s

    return pl.pallas_call(
        body, name=name,
        out_shape=jax.ShapeDtypeStruct((rows, cols), F32),
        grid=(2,),
        in_specs=[pl.BlockSpec((tr, cols), lambda i, d=d: (2 * d + i, 0)) for d in range(N_DEV)],
        out_specs=pl.BlockSpec((tr, cols), lambda i: (i, 0)),
        compiler_params=_params(dimension_semantics=("parallel",)),
    )(*([parts] * N_DEV))


def _adamw_step(w_ref, g_ref, m_ref, v_ref, d_ref, nm_ref, nv_ref):
    c1 = 1.0 / (1.0 - ADAM_B1 ** ADAM_STEP)
    c2 = 1.0 / (1.0 - ADAM_B2 ** ADAM_STEP)
    gv = g_ref[...]
    nm = ADAM_B1 * m_ref[...] + (1.0 - ADAM_B1) * gv
    nv = ADAM_B2 * v_ref[...] + (1.0 - ADAM_B2) * (gv * gv)
    nm_ref[...] = nm
    nv_ref[...] = nv
    d_ref[...] = (-ADAM_LR) * ((nm * c1) / (jnp.sqrt(nv * c2) + ADAM_EPS) + ADAM_WD * w_ref[...])


def _adamw_small(params, name):
    n = len(params)

    def body(*refs):
        for k in range(n):
            _adamw_step(*refs[4 * k:4 * k + 4], *refs[4 * n + 3 * k:4 * n + 3 * k + 3])

    out = pl.pallas_call(
        body, name=name,
        out_shape=tuple(jax.ShapeDtypeStruct(p[0].shape, F32) for p in params for _ in range(3)),
    )(*[a for p in params for a in p])
    return [out[3 * k:3 * k + 3] for k in range(n)]


def _adamw(w, g, m, v, name):
    rows, cols = w.shape
    tr = rows
    while tr * cols * 4 > (1 << 20) and tr % 16 == 0:
        tr //= 2

    def body(*refs):
        _adamw_step(*refs)

    spec = pl.BlockSpec((tr, cols), lambda i: (i, 0))
    shape = jax.ShapeDtypeStruct((rows, cols), F32)
    return pl.pallas_call(
        body, name=name,
        out_shape=(shape, shape, shape),
        grid=(rows // tr,),
        in_specs=[spec] * 4, out_specs=(spec,) * 3,
        compiler_params=_params(dimension_semantics=("parallel",)),
    )(w, g, m, v)


NAMES = ("ln1_g", "w_in", "b_in", "rpb", "w_att_o", "conv_w", "conv_b", "w_rg_a", "b_rg_a", "w_rg_i",
         "b_rg_i", "lru_lambda", "w_rec_o", "w_out", "ln2_g", "w_ff1", "w_ff2", "lnf_g")
TRANSPOSED = {"w_in": "w_in_t", "w_att_o": "w_att_o_t", "w_ff1": "w_ff1_t"}
ROW_SHARDED = ("w_rec_o", "w_out", "w_ff2")
REPLICATED = (("ln1_g", (1, D)), ("b_in", (1, D_IN)), ("rpb", (N_HEADS * N_RPB_R, N_RPB_C)),
              ("conv_b", (1, D_REC)), ("w_rg_a", (2 * N_REC_BLOCKS * REC_BLOCK, REC_BLOCK)),
              ("w_rg_i", (2 * N_REC_BLOCKS * REC_BLOCK, REC_BLOCK)), ("ln2_g", (1, D)), ("lnf_g", (1, D)))
GATE_BLOCKS = ("w_rg_a", "w_rg_i")
SMALL_ROWS = 112


def _chan_bits(vectors):
    chan = jnp.concatenate(vectors, axis=0)
    bits = lax.bitcast_convert_type(chan, BF16).reshape(-1)
    return jnp.pad(bits, (0, CHAN_BLOCK_ROWS * D - bits.shape[0])).reshape(CHAN_BLOCK_ROWS, D)


def _chan_from_bits(gathered):
    bits = gathered.reshape(N_DEV, CHAN_BLOCK_ROWS * D)[:, :2 * N_CHAN_ROWS * LANES]
    chan = lax.bitcast_convert_type(bits.reshape(N_DEV, N_CHAN_ROWS, LANES, 2), F32)
    return chan.transpose(1, 0, 2).reshape(N_CHAN_ROWS, D)


def kernel(x, ln1_g, w_in, b_in, rpb, w_att_o, conv_w, conv_b, w_rg_a, b_rg_a, w_rg_i, b_rg_i, lru_lambda, w_rec_o, w_out, ln2_g, w_ff1, w_ff2, lnf_g, loss_target, m_ln1_g, m_w_in, m_b_in, m_rpb, m_w_att_o, m_conv_w, m_conv_b, m_w_rg_a, m_b_rg_a, m_w_rg_i, m_b_rg_i, m_lru_lambda, m_w_rec_o, m_w_out, m_ln2_g, m_w_ff1, m_w_ff2, m_lnf_g, v_ln1_g, v_w_in, v_b_in, v_rpb, v_w_att_o, v_conv_w, v_conv_b, v_w_rg_a, v_b_rg_a, v_w_rg_i, v_b_rg_i, v_lru_lambda, v_w_rec_o, v_w_out, v_ln2_g, v_w_ff1, v_w_ff2, v_lnf_g):
    w = dict(zip(NAMES, (ln1_g, w_in, b_in, rpb, w_att_o, conv_w, conv_b, w_rg_a, b_rg_a, w_rg_i,
                         b_rg_i, lru_lambda, w_rec_o, w_out, ln2_g, w_ff1, w_ff2, lnf_g)))
    m = dict(zip(NAMES, (m_ln1_g, m_w_in, m_b_in, m_rpb, m_w_att_o, m_conv_w, m_conv_b, m_w_rg_a,
                         m_b_rg_a, m_w_rg_i, m_b_rg_i, m_lru_lambda, m_w_rec_o, m_w_out, m_ln2_g,
                         m_w_ff1, m_w_ff2, m_lnf_g)))
    v = dict(zip(NAMES, (v_ln1_g, v_w_in, v_b_in, v_rpb, v_w_att_o, v_conv_w, v_conv_b, v_w_rg_a,
                         v_b_rg_a, v_w_rg_i, v_b_rg_i, v_lru_lambda, v_w_rec_o, v_w_out, v_ln2_g,
                         v_w_ff1, v_w_ff2, v_lnf_g)))
    xi, yi, ci = _position()

    shard = {t: w[n][0].T.astype(BF16) for n, t in TRANSPOSED.items()}
    shard.update({n: w[n][0].astype(BF16) for n in ROW_SHARDED})
    shard["chan"] = _chan_bits([w[n][0] for n, _ in CHAN])
    first, later = ("w_in_t", "chan"), ("w_rec_o", "w_out", "w_att_o_t", "w_ff1_t", "w_ff2")
    *gathered, done = _all_gather([shard[n] for n in first], "weight_all_gather")
    p = dict(zip(first, gathered))
    send_sems, recv_sems, sent, zones, token = _gather_start([shard[n] for n in later], done,
                                                             "weight_gather_start")

    def late_weights(after):
        landed = _gather_wait(send_sems, recv_sems, sent, zones, after, "weight_gather_wait")
        return dict(zip(later, _gather_pass_on([shard[n].shape[0] for n in later], landed,
                                               "weight_gather_pass_on")))

    chan = _chan_from_bits(p.pop("chan"))
    r0 = 0
    for n, rows in CHAN:
        p[n] = chan[r0:r0 + rows]
        r0 += rows
    p.update(ln1_g=w["ln1_g"], b_in=w["b_in"] + token[0, 0], rpb=w["rpb"][0], conv_b=w["conv_b"],
             w_rg_a=w["w_rg_a"][0], w_rg_i=w["w_rg_i"][0], ln2_g=w["ln2_g"],
             lnf_g=w["lnf_g"].reshape(1, D))

    core = jnp.reshape(ci, (1,)).astype(jnp.int32)
    chip = jnp.reshape(2 * xi + yi, (1,)).astype(jnp.int32)
    early_sections, late_sections = SECTIONS[1:], SECTIONS[:1]
    in_flight = {}

    def pair_sum_and_send(group, sections, after):
        send_sems, recv_sems, sect, zones, _ = in_flight["pair_" + group]
        sect, got = _pair_exchange_wait(sections, send_sems, recv_sems, sect, zones, after,
                                        "grad_pair_exchange_wait_" + group)
        parts = _pair_add(sections, sect, got, core, "grad_pair_add_" + group)
        in_flight[group] = _chip_exchange_start(sections, parts, "grad_chip_exchange_start_" + group)
        return in_flight[group][-1]

    def reduce_early(grads):
        chan_g = jnp.concatenate([grads[n] for n, _ in CHAN], axis=0)
        chan_g = chan_g.reshape(N_CHAN_ROWS, N_DEV, LANES).transpose(1, 0, 2).astype(BF16)
        chan_g = jnp.pad(chan_g.reshape(N_DEV, -1), ((0, 0), (0, CHAN_BLOCK_ROWS * D - N_CHAN_ROWS * LANES)))
        grads["chan"] = chan_g.reshape(N_DEV * CHAN_BLOCK_ROWS, D)
        grads["gates"] = jnp.concatenate([grads[n].reshape(-1, D) for n in GATE_BLOCKS], axis=0).astype(BF16)
        in_flight["pair_early"] = _pair_exchange_start(
            early_sections, [grads[n] for n, _, _ in early_sections], PAIR_EARLY_ID,
            "grad_pair_exchange_start_early")
        return pair_sum_and_send("early", early_sections, in_flight["pair_early"][-1])[0, 0]

    loss_part, grad_x, grads = _local_step(x[0], loss_target[0], p, late_weights, reduce_early)
    in_flight["pair_late"] = _pair_exchange_start(
        late_sections, [grads[n] for n, _, _ in late_sections], PAIR_LATE_ID, "grad_pair_exchange_start_late")

    def finish(group, sections, after, name):
        send_sems, recv_sems, parts, zones, _ = in_flight[group]
        parts, far = _chip_exchange_wait(sections, send_sems, recv_sems, parts, zones, after,
                                         "grad_chip_exchange_wait_" + name)
        return dict(zip((n for n, _, _ in sections),
                        _grad_finish(sections, parts, far, chip, "grad_finish_" + name)))

    summed = finish("early", early_sections, in_flight["pair_late"][-1], "early")
    started_late = pair_sum_and_send("late", late_sections, summed["gates"])

    flat = jnp.concatenate([grads[n].reshape(-1) for n, _ in REPLICATED if n not in GATE_BLOCKS]
                           + [loss_part.reshape(-1) + started_late[0, 0]])
    n_small = flat.shape[0]
    flat = jnp.pad(flat, (0, SMALL_ROWS * LANES - n_small)).reshape(SMALL_ROWS, LANES)
    small_parts, gate_sum, _ = _all_gather([flat, summed["gates"]], "small_grad_all_gather")
    small = _sum_devices(small_parts, SMALL_ROWS, "small_grad_sum").reshape(-1)
    loss = small[n_small - 1]

    g, delta, new_m, new_v = {}, {}, {}, {}

    def update(n, g2, shape2):
        d2, m2, v2 = _adamw(w[n].reshape(shape2), g2, m[n].reshape(shape2), v[n].reshape(shape2),
                            "adamw_" + n)
        g[n], delta[n], new_m[n], new_v[n] = (a.reshape(w[n].shape) for a in (g2, d2, m2, v2))

    small_params = []
    o = 0
    for n, shape2 in REPLICATED:
        if n in GATE_BLOCKS:
            k, rows = GATE_BLOCKS.index(n), gate_sum.shape[0] // len(GATE_BLOCKS)
            update(n, gate_sum[k * rows:(k + 1) * rows].reshape(shape2), shape2)
        else:
            size = shape2[0] * shape2[1]
            small_params.append((n, small[o:o + size].reshape(shape2), shape2))
            o += size
    chan_back = summed["chan"].reshape(-1)[:N_CHAN_ROWS * LANES].reshape(N_CHAN_ROWS, LANES)
    r0 = 0
    for n, rows in CHAN:
        small_params.append((n, chan_back[r0:r0 + rows], (rows, LANES)))
        r0 += rows
    results = _adamw_small([(w[n].reshape(s2), g2, m[n].reshape(s2), v[n].reshape(s2))
                            for n, g2, s2 in small_params], "adamw_vectors")
    for (n, g2, _), (d2, m2, v2) in zip(small_params, results):
        g[n], delta[n], new_m[n], new_v[n] = (a.reshape(w[n].shape) for a in (g2, d2, m2, v2))

    for n in ROW_SHARDED:
        update(n, summed[n], summed[n].shape)
    for n, t in TRANSPOSED.items():
        if t in summed:
            update(n, summed[t].T, summed[t].shape[::-1])
    summed = finish("late", late_sections, _after_all(list(delta.values()), "updates_done"), "late")
    update("w_in", summed["w_in_t"].T, summed["w_in_t"].shape[::-1])

    return (loss, grad_x[None], *[g[n] for n in NAMES], *[delta[n] for n in NAMES],
            *[new_m[n] for n in NAMES], *[new_v[n] for n in NAMES])
```

```python
import math

import numpy as np
import jax
import jax.numpy as jnp
from jax import lax
from jax.experimental import pallas as pl
from jax.experimental.pallas import tpu as pltpu

F32 = jnp.float32
BF16 = jnp.bfloat16

T = 2048
D = 1024
D_ATT = 512
D_REC = 1024
D_FF = 4096
D_IN = 5632
N_HEADS = 8
DH = 64
GRID_W = 64
ROWS = T // GRID_W
WIN_H = 8
WIN_W = 16
KWIN = WIN_H * GRID_W
N_RPB_R = 2 * WIN_H - 1
N_RPB_C = 2 * WIN_W - 1
N_REC_BLOCKS = 16
REC_BLOCK = 64
CG = 128
N_CG = D_REC // CG
LRU_C = 8.0
EPS = 1e-6
N_DEV = 8
N_CHIPS = 4
LANES = 128

ADAM_LR = 0.001
ADAM_B1 = 0.9
ADAM_B2 = 0.999
ADAM_EPS = 1e-08
ADAM_WD = 0.01
ADAM_STEP = 10

MESH_AXES = ("x", "y", "c")
VMEM_LIMIT = 56 * 1024 * 1024

TILE = 512
DZ_ARRAYS = ((0, 3, 1), (3, 4, 2), (7, 4, 2))
N_DZ_TILES = D_IN // TILE


def _params(**kw):
    return pltpu.CompilerParams(vmem_limit_bytes=VMEM_LIMIT, **kw)


HG = 4
HQ = HG * GRID_W
HC = HG * DH


def _att_tables():
    rq = np.arange(GRID_W)
    kc = np.arange(KWIN) % GRID_W
    win_start = np.clip(rq - WIN_W // 2, 0, GRID_W - WIN_W)
    valid = (kc[None, :] >= win_start[:, None]) & (kc[None, :] < win_start[:, None] + WIN_W)
    same_head = (np.arange(HQ)[:, None] // GRID_W) == (np.arange(HC)[None, :] // DH)
    return valid.astype(np.float32), same_head.astype(np.float32)


def _pair_mask():
    half = np.arange(2 * DH) // DH
    return (half[:, None] == half[None, :]).astype(np.float32)


def _dup_table():
    return np.concatenate([np.eye(REC_BLOCK, dtype=np.float32)] * 2, axis=1)


def _sigmoid(x):
    return 0.5 * jnp.tanh(0.5 * x) + 0.5


def _softplus(x):
    return jnp.maximum(x, 0.0) + jnp.log(1.0 + jnp.exp(-jnp.abs(x)))


def _one_minus_square(log_a, a):
    x = 2.0 * log_a
    series = -x * (1.0 + x * (0.5 + x * (1.0 / 6.0)))
    return jnp.where(x > -0.02, series, 1.0 - a * a)


_GELU_C = math.sqrt(2.0 / math.pi)


def _gelu_and_grad(x):
    x2 = x * x
    inner = _GELU_C * (x + 0.044715 * x * x2)
    t = jnp.tanh(inner)
    g = 0.5 * x * (1.0 + t)
    dg = 0.5 * (1.0 + t) + 0.5 * x * (1.0 - t * t) * _GELU_C * (1.0 + 3.0 * 0.044715 * x2)
    return g, dg


def _dot(a, b):
    return jnp.dot(a, b, preferred_element_type=F32)


def _dot_nt(a, b):
    return lax.dot_general(a, b, (((1,), (1,)), ((), ())), preferred_element_type=F32)


def _dot_tn(a, b):
    return lax.dot_general(a, b, (((0,), (0,)), ((), ())), preferred_element_type=F32)


def _dot_exact(a, b):
    return jnp.dot(a, b, precision=lax.Precision.HIGHEST, preferred_element_type=F32)


def _shift_rows(x, s):
    n = x.shape[0]
    rows = lax.broadcasted_iota(jnp.int32, x.shape, 0)
    y = pltpu.roll(x, s % n, 0)
    if s > 0:
        return jnp.where(rows >= s, y, 0.0)
    return jnp.where(rows < n + s, y, 0.0)


def _rms_bwd(dh, xh, r, g):
    dxh = dh * g
    return r * (dxh - xh * jnp.mean(dxh * xh, axis=-1, keepdims=True))


def _matmul(a, b, mode, out_dtype, name, tm=512, tn=1024, tk=2048):
    if mode == "nn":
        (m, k), (k2, n) = a.shape, b.shape
    elif mode == "nt":
        (m, k), (n, k2) = a.shape, b.shape
    else:
        (k, m), (k2, n) = a.shape, b.shape
    assert k == k2
    tm, tn, tk = min(tm, m), min(tn, n), min(tk, k)
    assert m % tm == 0 and n % tn == 0 and k % tk == 0
    nk = k // tk
    dot = {"nn": _dot, "nt": _dot_nt, "tn": _dot_tn}[mode]

    def body(a_ref, b_ref, o_ref, acc):
        kk = pl.program_id(2)
        part = dot(a_ref[...].astype(BF16), b_ref[...].astype(BF16))
        if nk == 1:
            o_ref[...] = part.astype(out_dtype)
            return

        @pl.when(kk == 0)
        def _():
            acc[...] = part

        @pl.when(kk > 0)
        def _():
            acc[...] += part

        @pl.when(kk == nk - 1)
        def _():
            o_ref[...] = acc[...].astype(out_dtype)

    if mode == "tn":
        a_spec = pl.BlockSpec((tk, tm), lambda i, j, kk: (kk, i))
    else:
        a_spec = pl.BlockSpec((tm, tk), lambda i, j, kk: (i, kk))
    if mode == "nt":
        b_spec = pl.BlockSpec((tn, tk), lambda i, j, kk: (j, kk))
    else:
        b_spec = pl.BlockSpec((tk, tn), lambda i, j, kk: (kk, j))
    return pl.pallas_call(
        body, name=name,
        out_shape=jax.ShapeDtypeStruct((m, n), out_dtype),
        grid=(m // tm, n // tn, nk),
        in_specs=[a_spec, b_spec],
        out_specs=pl.BlockSpec((tm, tn), lambda i, j, kk: (i, j)),
        scratch_shapes=[pltpu.VMEM((tm, tn) if nk > 1 else (8, LANES), F32)],
        compiler_params=_params(dimension_semantics=("parallel", "parallel", "arbitrary")),
    )(a, b)


def _in_proj(x, g1, w_in_t, b_in):
    tm = 1024

    def body(x_ref, g_ref, w_ref, b_ref, qkv_ref, uy_ref, gg_ref, h_ref, h_scr):
        j = pl.program_id(1)

        @pl.when(j == 0)
        def _():
            xv = x_ref[...]
            r = lax.rsqrt(jnp.mean(xv * xv, axis=-1, keepdims=True) + EPS)
            h = ((xv * r) * g_ref[...]).astype(BF16)
            h_scr[...] = h
            h_ref[...] = h

        z = _dot_nt(h_scr[...], w_ref[...]) + b_ref[...]

        @pl.when(j < 3)
        def _():
            qkv_ref[...] = z.astype(BF16)

        @pl.when((j >= 3) & (j < 7))
        def _():
            uy_ref[...] = z

        @pl.when(j >= 7)
        def _():
            gg_ref[...] = z

    return pl.pallas_call(
        body, name="in_proj",
        out_shape=(jax.ShapeDtypeStruct((T, 3 * D_ATT), BF16),
                   jax.ShapeDtypeStruct((T, 2 * D_REC), F32),
                   jax.ShapeDtypeStruct((T, 2 * D), F32),
                   jax.ShapeDtypeStruct((T, D), BF16)),
        grid=(T // tm, N_DZ_TILES),
        in_specs=[pl.BlockSpec((tm, D), lambda i, j: (i, 0)),
                  pl.BlockSpec((1, D), lambda i, j: (0, 0)),
                  pl.BlockSpec((TILE, D), lambda i, j: (j, 0)),
                  pl.BlockSpec((1, TILE), lambda i, j: (0, j))],
        out_specs=(pl.BlockSpec((tm, TILE), lambda i, j: (i, jnp.minimum(j, 2))),
                   pl.BlockSpec((tm, TILE), lambda i, j: (i, jnp.clip(j - 3, 0, 3))),
                   pl.BlockSpec((tm, TILE), lambda i, j: (i, jnp.clip(j - 7, 0, 3))),
                   pl.BlockSpec((tm, D), lambda i, j: (i, 0))),
        scratch_shapes=[pltpu.VMEM((tm, D), BF16)],
        compiler_params=_params(dimension_semantics=("parallel", "arbitrary")),
    )(x, g1, w_in_t, b_in)


def _dz_specs(rows, tile_of, row_of):
    def spec(off, n, per_plane):
        def index(*ids):
            t = jnp.clip(tile_of(*ids) - off, 0, n - 1)
            return (t // per_plane, row_of(*ids), t % per_plane)
        return pl.BlockSpec((1, rows, TILE), index)
    return [spec(off, n, per) for off, n, per in DZ_ARRAYS]


def _dh_norm1_bwd(dz, w_in_t, x, g1, dx1):
    tm = 1024

    def body(*refs):
        seg_refs = refs[:3]
        w_ref, x_ref, g_ref, dx1_ref, gx_ref, dg_ref, acc = refs[3:]
        i, kk = pl.program_id(0), pl.program_id(1)

        @pl.when(kk == 0)
        def _():
            acc[...] = jnp.zeros_like(acc)

        for s, (off, n, _) in enumerate(DZ_ARRAYS):
            @pl.when((kk >= off) & (kk < off + n))
            def _(s=s):
                acc[...] += _dot(seg_refs[s][0], w_ref[...])

        @pl.when((i == 0) & (kk == 0))
        def _():
            dg_ref[...] = jnp.zeros_like(dg_ref)

        @pl.when(kk == N_DZ_TILES - 1)
        def _():
            xv = x_ref[...]
            r = lax.rsqrt(jnp.mean(xv * xv, axis=-1, keepdims=True) + EPS)
            xh = xv * r
            dh = acc[...]
            dg_ref[...] += jnp.sum(dh * xh, axis=0, keepdims=True)
            gx_ref[...] = dx1_ref[...] + _rms_bwd(dh, xh, r, g_ref[...])

    tok = pl.BlockSpec((tm, D), lambda i, j: (i, 0))
    vec = pl.BlockSpec((1, D), lambda i, j: (0, 0))
    return pl.pallas_call(
        body, name="dh_norm1_bwd",
        out_shape=(jax.ShapeDtypeStruct((T, D), F32), jax.ShapeDtypeStruct((1, D), F32)),
        grid=(T // tm, N_DZ_TILES),
        in_specs=_dz_specs(tm, lambda i, j: j, lambda i, j: i)
        + [pl.BlockSpec((TILE, D), lambda i, j: (j, 0)), tok, vec, tok],
        out_specs=(tok, vec),
        scratch_shapes=[pltpu.VMEM((tm, D), F32)],
        compiler_params=_params(dimension_semantics=("arbitrary", "arbitrary")),
    )(*dz, w_in_t, x, g1, dx1)


def _grad_w_in(dz, h):
    def body(*refs):
        seg_refs = refs[:3]
        h_ref, gw_ref, gb_ref = refs[3:]
        j = pl.program_id(0)

        for s, (off, n, _) in enumerate(DZ_ARRAYS):
            @pl.when((j >= off) & (j < off + n))
            def _(s=s):
                a = seg_refs[s][0]
                gw_ref[...] = _dot_tn(a, h_ref[...]).astype(BF16)
                gb_ref[...] = jnp.sum(a.astype(F32), axis=0, keepdims=True)

    return pl.pallas_call(
        body, name="grad_w_in",
        out_shape=(jax.ShapeDtypeStruct((D_IN, D), BF16), jax.ShapeDtypeStruct((1, D_IN), F32)),
        grid=(N_DZ_TILES,),
        in_specs=_dz_specs(T, lambda j: j, lambda j: 0) + [pl.BlockSpec((T, D), lambda j: (0, 0))],
        out_specs=(pl.BlockSpec((TILE, D), lambda j: (j, 0)), pl.BlockSpec((1, TILE), lambda j: (0, j))),
        compiler_params=_params(dimension_semantics=("parallel",)),
    )(*dz, h)


def _rpb_rows(rpb):
    padded = jnp.pad(rpb, ((0, 0), (0, 0), (0, GRID_W - N_RPB_C)))
    rows = [padded[:, WIN_H - 1 - oi: 2 * WIN_H - 1 - oi].reshape(N_HEADS // HG, HG, KWIN)
            for oi in range(WIN_H)]
    return jnp.stack(rows, axis=0)


SKEW = KWIN - (WIN_W - 1)


MASKED = -1e30


def _bias_tiles(rows_ref, valid, bias_s):
    for oi in range(WIN_H):
        for hh in range(HG):
            row = jnp.broadcast_to(rows_ref[oi, 0, hh:hh + 1, :], (GRID_W, KWIN))
            tile = pltpu.roll(row, SKEW, 1, stride=1, stride_axis=0)
            bias_s[oi, hh * GRID_W:(hh + 1) * GRID_W, :] = jnp.where(valid, tile, MASKED)


def _bias_tile_grads(gb_s, flip, out_ref):
    for oi in range(WIN_H):
        for hh in range(HG):
            g = _dot_exact(flip, gb_s[oi, hh * GRID_W:(hh + 1) * GRID_W, :])
            back = pltpu.roll(g, KWIN - (GRID_W - WIN_W), 1, stride=1, stride_axis=0)
            out_ref[0, oi, hh:hh + 1, :] = jnp.sum(back, axis=0, keepdims=True)


def _rpb_fold(row_grads):
    g = row_grads.transpose(1, 0, 2, 3).reshape(WIN_H, N_HEADS, WIN_H, GRID_W)
    g = g.transpose(0, 2, 1, 3)

    def body(g_ref, o_ref):
        for dr in range(N_RPB_R):
            terms = [g_ref[oi, i] for oi in range(WIN_H) for i in range(WIN_H) if i - oi + WIN_H - 1 == dr]
            acc = terms[0]
            for term in terms[1:]:
                acc = acc + term
            o_ref[dr] = acc

    out = pl.pallas_call(
        body, name="rpb_fold",
        out_shape=jax.ShapeDtypeStruct((N_RPB_R, N_HEADS, GRID_W), F32),
    )(g)
    return out.transpose(1, 0, 2)[:, :, :N_RPB_C]


ATT_GROUPS = N_HEADS // HG
ATT_UNROLL = 2


def _stacked(rows64, same_head):
    return jnp.where(same_head, jnp.concatenate([rows64] * HG, axis=0), jnp.zeros((), BF16))


def _own_heads(stacked):
    head = lax.broadcasted_iota(jnp.int32, (GRID_W, HC), 1) // DH
    out = stacked[:GRID_W]
    for h in range(1, HG):
        out = jnp.where(head == h, stacked[h * GRID_W:(h + 1) * GRID_W], out)
    return out


def _att_scores(q_ref, k_ref, bias_ref, same_head, r):
    rs = jnp.clip(r - WIN_H // 2, 0, ROWS - WIN_H)
    oi = r - rs
    q0 = pl.multiple_of(r * GRID_W, GRID_W)
    k0 = pl.multiple_of(rs * GRID_W, GRID_W)
    q2 = _stacked(q_ref[pl.ds(q0, GRID_W), :] * (DH ** -0.5), same_head)
    kw = k_ref[pl.ds(k0, KWIN), :]
    s = _dot_nt(q2, kw) + bias_ref[oi]
    e = jnp.exp(s - jnp.max(s, axis=-1, keepdims=True))
    return e, 1.0 / jnp.sum(e, axis=-1, keepdims=True), q2, kw, q0, k0, oi


def _att_specs():
    col = lambda off: pl.BlockSpec((T, HC), lambda g: (0, g + off * ATT_GROUPS))
    tables = [pl.BlockSpec((WIN_H, 1, HG, KWIN), lambda g: (0, g, 0, 0)),
              pl.BlockSpec((GRID_W, KWIN), lambda g: (0, 0)),
              pl.BlockSpec((HQ, HC), lambda g: (0, 0))]
    return col, tables, pltpu.VMEM((WIN_H, HQ, KWIN), F32)


def _att_fwd(qkv, bias_rows):
    valid_np, same_head_np = _att_tables()

    def body(q_ref, k_ref, v_ref, rows_ref, valid_ref, head_ref, o_ref, bias_s):
        same_head = head_ref[...] > 0.5
        _bias_tiles(rows_ref, valid_ref[...] > 0.5, bias_s)

        def row(r, carry):
            e, rl, _, _, q0, k0, _ = _att_scores(q_ref, k_ref, bias_s, same_head, r)
            o2 = _dot((e * rl).astype(BF16), v_ref[pl.ds(k0, KWIN), :])
            o_ref[pl.ds(q0, GRID_W), :] = _own_heads(o2).astype(BF16)
            return carry

        lax.fori_loop(0, ROWS, row, 0, unroll=ATT_UNROLL)

    col, tables, tiles = _att_specs()
    return pl.pallas_call(
        body, name="att_fwd",
        out_shape=jax.ShapeDtypeStruct((T, D_ATT), BF16),
        grid=(ATT_GROUPS,),
        in_specs=[col(0), col(1), col(2)] + tables,
        out_specs=col(0),
        scratch_shapes=[tiles],
        compiler_params=_params(dimension_semantics=("parallel",)),
    )(qkv, qkv, qkv, bias_rows, jnp.asarray(valid_np), jnp.asarray(same_head_np))


def _att_bwd(qkv, bias_rows, datt, after):
    valid_np, same_head_np = _att_tables()

    def body(q_ref, k_ref, v_ref, do_ref, rows_ref, valid_ref, head_ref, flip_ref,
             dqkv_ref, grows_ref, dk_acc, dv_acc, bias_s, gb_s):
        same_head = head_ref[...] > 0.5
        dk_acc[...] = jnp.zeros_like(dk_acc)
        dv_acc[...] = jnp.zeros_like(dv_acc)
        gb_s[...] = jnp.zeros_like(gb_s)
        _bias_tiles(rows_ref, valid_ref[...] > 0.5, bias_s)

        def row(r, carry):
            e, rl, q2, kw, q0, k0, oi = _att_scores(q_ref, k_ref, bias_s, same_head, r)
            do2 = _stacked(do_ref[pl.ds(q0, GRID_W), :], same_head)
            vw = v_ref[pl.ds(k0, KWIN), :]
            p = e * rl
            dp = _dot_nt(do2, vw)
            ds = p * (dp - jnp.sum(dp * p, axis=-1, keepdims=True))
            p16 = p.astype(BF16)
            ds16 = ds.astype(BF16)
            dv_acc[pl.ds(k0, KWIN), :] += _dot_tn(p16, do2)
            dk_acc[pl.ds(k0, KWIN), :] += _dot_tn(ds16, q2)
            dq2 = _dot(ds16, kw) * (DH ** -0.5)
            dqkv_ref[0, pl.ds(q0, GRID_W), :] = _own_heads(dq2).astype(BF16)
            gb_s[oi] += ds
            return carry

        lax.fori_loop(0, ROWS, row, 0, unroll=ATT_UNROLL)
        dqkv_ref[1] = dk_acc[...].astype(BF16)
        dqkv_ref[2] = dv_acc[...].astype(BF16)
        _bias_tile_grads(gb_s, flip_ref[...], grows_ref)

    col, tables, tiles = _att_specs()
    return pl.pallas_call(
        body, name="att_bwd",
        out_shape=(jax.ShapeDtypeStruct((3, T, D_ATT), BF16),
                   jax.ShapeDtypeStruct((ATT_GROUPS, WIN_H, HG, KWIN), F32)),
        grid=(ATT_GROUPS,),
        in_specs=[col(0), col(1), col(2), col(0)] + tables + [pl.BlockSpec((GRID_W, GRID_W), lambda g: (0, 0))],
        out_specs=(pl.BlockSpec((3, T, HC), lambda g: (0, 0, g)),
                   pl.BlockSpec((1, WIN_H, HG, KWIN), lambda g: (g, 0, 0, 0))),
        scratch_shapes=[pltpu.VMEM((T, HC), F32), pltpu.VMEM((T, HC), F32), tiles, tiles],
        compiler_params=_params(dimension_semantics=("parallel",)),
    )(qkv, qkv, qkv, datt, bias_rows, jnp.asarray(valid_np) + after, jnp.asarray(same_head_np),
      jnp.asarray(np.eye(GRID_W, dtype=np.float32)[::-1].copy()))


def _conv_taps(up):
    return (_shift_rows(up, 2), _shift_rows(up, 1), up, _shift_rows(up, -1))


def _pair_block_diag(w_pair, dup, same_half):
    return jnp.where(same_half, _dot(w_pair.astype(BF16), dup), 0.0).astype(BF16)


def _gates(u, u16, wa, ba, wi, bi, lam):
    r = _sigmoid(_dot(u16, wa) + ba)
    ig = _sigmoid(_dot(u16, wi) + bi)
    sp = _softplus(-lam)
    log_a = (-LRU_C) * r * sp
    a = jnp.exp(log_a)
    mult2 = jnp.maximum(_one_minus_square(log_a, a), 0.0)
    return r, ig, sp, a, jnp.sqrt(mult2), mult2


SCAN_BLOCKS = 2


def _scans(jobs):
    c = jobs[0][0].shape[1]
    nblk = T // 8
    rows = lax.broadcasted_iota(jnp.int32, (8, c), 0)

    def block(a, b, reverse):
        for s in (1, 2, 4):
            if reverse:
                keep = rows < 8 - s
                a_s = jnp.where(keep, pltpu.roll(a, 8 - s, 0), 1.0)
                b_s = jnp.where(keep, pltpu.roll(b, 8 - s, 0), 0.0)
            else:
                keep = rows >= s
                a_s = jnp.where(keep, pltpu.roll(a, s, 0), 1.0)
                b_s = jnp.where(keep, pltpu.roll(b, s, 0), 0.0)
            b = a * b_s + b
            a = a * a_s
        return a, b

    def step(i, carry):
        out = []
        for (a_ref, b_ref, h_ref, reverse), h_prev in zip(jobs, carry):
            for u in range(SCAN_BLOCKS):
                blk = i * SCAN_BLOCKS + u
                if reverse:
                    blk = nblk - 1 - blk
                t0 = pl.multiple_of(blk * 8, 8)
                a, b = block(a_ref[pl.ds(t0, 8), :], b_ref[pl.ds(t0, 8), :], reverse)
                h = a * h_prev + b
                h_ref[pl.ds(t0, 8), :] = h
                h_prev = jnp.broadcast_to(h[0:1] if reverse else h[7:8], (8, c))
            out.append(h_prev)
        return tuple(out)

    lax.fori_loop(0, nblk // SCAN_BLOCKS, step, tuple(jnp.zeros((8, c), F32) for _ in jobs))


def _rec_specs():
    tok = lambda off: pl.BlockSpec((T, CG), lambda g: (0, g + off))
    per_ch = lambda rows: pl.BlockSpec((rows, CG), lambda g: (0, g))
    wspec = pl.BlockSpec((2, 1, CG, REC_BLOCK), lambda g: (0, g, 0, 0))
    const = lambda shape: pl.BlockSpec(shape, lambda g: (0, 0))
    return tok, per_ch, wspec, const


def _rec_fwd(uy, conv_w, conv_b, w_a, b_a, w_i, b_i, lam):
    tok, per_ch, wspec, const = _rec_specs()

    def body(up_ref, yb_ref, cw_ref, cb_ref, wa_ref, ba_ref, wi_ref, bi_ref, lam_ref, dup_ref, half_ref,
             hf_ref, hb_ref, yrec_ref, a_f, bx_f, a_b, bx_b):
        dup = dup_ref[...]
        same_half = half_ref[...] > 0.5
        taps = _conv_taps(up_ref[...])
        u = cb_ref[...]
        for j in range(4):
            u = u + taps[j] * cw_ref[j:j + 1, :]
        u16 = u.astype(BF16)
        for d, (a_s, bx_s) in enumerate(((a_f, bx_f), (a_b, bx_b))):
            wa = _pair_block_diag(wa_ref[d, 0], dup, same_half)
            wi = _pair_block_diag(wi_ref[d, 0], dup, same_half)
            _, ig, _, a, mult, _ = _gates(u, u16, wa, ba_ref[d:d + 1, :], wi, bi_ref[d:d + 1, :],
                                       lam_ref[d:d + 1, :])
            a_s[...] = a
            bx_s[...] = mult * (ig * u)
        _scans([(a_f, bx_f, hf_ref, False), (a_b, bx_b, hb_ref, True)])
        gelu, _ = _gelu_and_grad(yb_ref[...])
        yrec_ref[...] = ((hf_ref[...] + hb_ref[...]) * gelu).astype(BF16)

    return pl.pallas_call(
        body, name="rec_fwd",
        out_shape=(jax.ShapeDtypeStruct((T, D_REC), F32), jax.ShapeDtypeStruct((T, D_REC), F32),
                   jax.ShapeDtypeStruct((T, D_REC), BF16)),
        grid=(N_CG,),
        in_specs=[tok(0), tok(N_CG), per_ch(4), per_ch(1), wspec, per_ch(2), wspec, per_ch(2), per_ch(2),
                  const((REC_BLOCK, CG)), const((CG, CG))],
        out_specs=(tok(0), tok(0), tok(0)),
        scratch_shapes=[pltpu.VMEM((T, CG), F32)] * 4,
        compiler_params=_params(dimension_semantics=("parallel",)),
    )(uy, uy, conv_w, conv_b, w_a, b_a, w_i, b_i, lam,
      jnp.asarray(_dup_table(), BF16), jnp.asarray(_pair_mask()))


def _rec_bwd(uy, hf, hb, dyrec, conv_w, conv_b, w_a, b_a, w_i, b_i, lam):
    tok, per_ch, wspec, const = _rec_specs()

    def body(up_ref, yb_ref, hf_ref, hb_ref, dy_ref, cw_ref, cb_ref, wa_ref, ba_ref, wi_ref, bi_ref,
             lam_ref, dup_ref, dupt_ref, half_ref,
             duy_ref, dcw_ref, dcb_ref, dwa_ref, dba_ref, dwi_ref, dbi_ref, dlam_ref,
             a_s0, a_s1, dh_s, g_s0, g_s1):
        dup = dup_ref[...]
        dup_t = dupt_ref[...]
        same_half = half_ref[...] > 0.5
        taps = _conv_taps(up_ref[...])
        u = cb_ref[...]
        for j in range(4):
            u = u + taps[j] * cw_ref[j:j + 1, :]
        u16 = u.astype(BF16)
        gelu, dgelu = _gelu_and_grad(yb_ref[...])
        dy = dy_ref[...]
        duy_ref[1] = (dy * (hf_ref[...] + hb_ref[...]) * dgelu).astype(BF16)
        dh_s[...] = dy * gelu
        gate_values = []
        for d, a_s in enumerate((a_s0, a_s1)):
            wa = _pair_block_diag(wa_ref[d, 0], dup, same_half)
            wi = _pair_block_diag(wi_ref[d, 0], dup, same_half)
            lam_d = lam_ref[d:d + 1, :]
            r, ig, sp, a, mult, mult2 = _gates(u, u16, wa, ba_ref[d:d + 1, :], wi, bi_ref[d:d + 1, :], lam_d)
            a_s[...] = _shift_rows(a, 1 if d == 1 else -1)
            gate_values.append((wa, wi, lam_d, r, ig, sp, a, mult, mult2))
        _scans([(a_s0, dh_s, g_s0, True), (a_s1, dh_s, g_s1, False)])
        du = jnp.zeros((T, CG), F32)
        for d, g_s in enumerate((g_s0, g_s1)):
            reverse = d == 1
            wa, wi, lam_d, r, ig, sp, a, mult, mult2 = gate_values[d]
            g = g_s[...]
            h_prev = _shift_rows(hb_ref[...], -1) if reverse else _shift_rows(hf_ref[...], 1)
            da = g * h_prev
            dmult = g * (ig * u)
            dig = g * mult * u
            du = du + g * mult * ig
            dmult_dlog = jnp.where(mult2 > 0.0, -(a * a) * lax.rsqrt(mult2), 0.0)
            dlog_a = da * a + dmult * dmult_dlog
            dr = dlog_a * ((-LRU_C) * sp)
            dsp = jnp.sum(dlog_a * ((-LRU_C) * r), axis=0, keepdims=True)
            dlam_ref[d:d + 1, :] = dsp * (-_sigmoid(-lam_d))
            dga = dr * r * (1.0 - r)
            dgi = dig * ig * (1.0 - ig)
            dga16 = dga.astype(BF16)
            dgi16 = dgi.astype(BF16)
            du = du + _dot_nt(dga16, wa) + _dot_nt(dgi16, wi)
            dwa_ref[d, 0] = _dot_exact(jnp.where(same_half, _dot_tn(u16, dga16), 0.0), dup_t)
            dwi_ref[d, 0] = _dot_exact(jnp.where(same_half, _dot_tn(u16, dgi16), 0.0), dup_t)
            dba_ref[d:d + 1, :] = jnp.sum(dga, axis=0, keepdims=True)
            dbi_ref[d:d + 1, :] = jnp.sum(dgi, axis=0, keepdims=True)
        dcb_ref[...] = jnp.sum(du, axis=0, keepdims=True)
        for j in range(4):
            dcw_ref[j:j + 1, :] = jnp.sum(du * taps[j], axis=0, keepdims=True)
        dup_in = (_shift_rows(du, -2) * cw_ref[0:1, :] + _shift_rows(du, -1) * cw_ref[1:2, :]
                  + du * cw_ref[2:3, :] + _shift_rows(du, 1) * cw_ref[3:4, :])
        duy_ref[0] = dup_in.astype(BF16)

    wshape = jax.ShapeDtypeStruct((2, N_CG, CG, REC_BLOCK), F32)
    vec = lambda rows: jax.ShapeDtypeStruct((rows, D_REC), F32)
    dup_np = _dup_table()
    return pl.pallas_call(
        body, name="rec_bwd",
        out_shape=(jax.ShapeDtypeStruct((2, T, D_REC), BF16),
                   vec(4), vec(1), wshape, vec(2), wshape, vec(2), vec(2)),
        grid=(N_CG,),
        in_specs=[tok(0), tok(N_CG), tok(0), tok(0), tok(0),
                  per_ch(4), per_ch(1), wspec, per_ch(2), wspec, per_ch(2), per_ch(2),
                  const((REC_BLOCK, CG)), const((CG, REC_BLOCK)), const((CG, CG))],
        out_specs=(pl.BlockSpec((2, T, CG), lambda g: (0, 0, g)),
                   per_ch(4), per_ch(1), wspec, per_ch(2), wspec, per_ch(2), per_ch(2)),
        scratch_shapes=[pltpu.VMEM((T, CG), F32)] * 5,
        compiler_params=_params(dimension_semantics=("parallel",)),
    )(uy, uy, hf, hb, dyrec, conv_w, conv_b, w_a, b_a, w_i, b_i, lam,
      jnp.asarray(dup_np, BF16), jnp.asarray(dup_np.T.copy()), jnp.asarray(_pair_mask()))


TM_MIX = 256


def _mix_specs():
    tok = lambda width, blk=0: pl.BlockSpec((TM_MIX, width), lambda i: (i, blk))
    full = lambda shape: pl.BlockSpec(shape, lambda i: (0, 0))
    return tok, full


def _mix_fwd(x, att, yrec, gg, w_att_o_t, w_rec_o, w_out):
    tok, full = _mix_specs()

    def body(x_ref, att_ref, yr_ref, ga_ref, gr_ref, wao_ref, wro_ref, wo_ref, x1_ref, mixed_ref):
        y_att = _dot_nt(att_ref[...], wao_ref[...])
        y_rec = _dot(yr_ref[...], wro_ref[...])
        mixed = (_sigmoid(ga_ref[...]) * y_att + _sigmoid(gr_ref[...]) * y_rec).astype(BF16)
        mixed_ref[...] = mixed
        x1_ref[...] = x_ref[...] + _dot(mixed, wo_ref[...])

    return pl.pallas_call(
        body, name="mix_fwd",
        out_shape=(jax.ShapeDtypeStruct((T, D), F32), jax.ShapeDtypeStruct((T, D), BF16)),
        grid=(T // TM_MIX,),
        in_specs=[tok(D), tok(D_ATT), tok(D_REC), tok(D, 0), tok(D, 1),
                  full((D, D_ATT)), full((D_REC, D)), full((D, D))],
        out_specs=(tok(D), tok(D)),
        compiler_params=_params(dimension_semantics=("parallel",)),
    )(x, att, yrec, gg, gg, w_att_o_t, w_rec_o, w_out)


def _mix_bwd(dx1, att, yrec, gg, w_att_o_t, w_rec_o, w_out):
    tok, full = _mix_specs()

    def body(dx_ref, att_ref, yr_ref, ga_ref, gr_ref, wao_ref, wro_ref, wo_ref,
             dgg_ref, dya_ref, dyr_ref, datt_ref, dyrp_ref):
        dmixed = _dot_nt(dx_ref[...].astype(BF16), wo_ref[...])
        y_att = _dot_nt(att_ref[...], wao_ref[...])
        y_rec = _dot(yr_ref[...], wro_ref[...])
        sa = _sigmoid(ga_ref[...])
        sr = _sigmoid(gr_ref[...])
        dgg_ref[0] = (dmixed * y_att * sa * (1.0 - sa)).astype(BF16)
        dgg_ref[1] = (dmixed * y_rec * sr * (1.0 - sr)).astype(BF16)
        dya = (dmixed * sa).astype(BF16)
        dyr = (dmixed * sr).astype(BF16)
        dya_ref[...] = dya
        dyr_ref[...] = dyr
        datt_ref[...] = _dot(dya, wao_ref[...]).astype(BF16)
        dyrp_ref[...] = _dot_nt(dyr, wro_ref[...])

    return pl.pallas_call(
        body, name="mix_bwd",
        out_shape=(jax.ShapeDtypeStruct((2, T, D), BF16),
                   jax.ShapeDtypeStruct((T, D), BF16), jax.ShapeDtypeStruct((T, D), BF16),
                   jax.ShapeDtypeStruct((T, D_ATT), BF16), jax.ShapeDtypeStruct((T, D_REC), F32)),
        grid=(T // TM_MIX,),
        in_specs=[tok(D), tok(D_ATT), tok(D_REC), tok(D, 0), tok(D, 1),
                  full((D, D_ATT)), full((D_REC, D)), full((D, D))],
        out_specs=(pl.BlockSpec((2, TM_MIX, D), lambda i: (0, i, 0)),
                   tok(D), tok(D), tok(D_ATT), tok(D_REC)),
        compiler_params=_params(dimension_semantics=("parallel",)),
    )(dx1, att, yrec, gg, gg, w_att_o_t, w_rec_o, w_out)


TM_FFN = 256
FF_CHUNK = 1024


def _ffn_loss(x1, target, g2, gf, w_ff1_t, w_ff2):
    n_chunks = D_FF // FF_CHUNK

    def body(x1_ref, tg_ref, g2_ref, gf_ref, w1_hbm, w2_hbm,
             loss_ref, dx1_ref, h2_ref, act_ref, dpre_ref, dx2_ref, dg2_ref, dgf_ref,
             w1, w2, relu_s):
        i = pl.program_id(0)

        @pl.when(i == 0)
        def _():
            pltpu.sync_copy(w1_hbm, w1)
            pltpu.sync_copy(w2_hbm, w2)
            loss_ref[...] = jnp.zeros_like(loss_ref)
            dg2_ref[...] = jnp.zeros_like(dg2_ref)
            dgf_ref[...] = jnp.zeros_like(dgf_ref)

        x1v = x1_ref[...]
        r2 = lax.rsqrt(jnp.mean(x1v * x1v, axis=-1, keepdims=True) + EPS)
        xh2 = x1v * r2
        h2 = (xh2 * g2_ref[...]).astype(BF16)
        h2_ref[...] = h2
        x2 = x1v
        for c in range(n_chunks):
            ff = slice(c * FF_CHUNK, (c + 1) * FF_CHUNK)
            rl = jnp.maximum(_dot_nt(h2, w1[ff, :]), 0.0)
            relu_s[:, ff] = rl
            act = (rl * rl).astype(BF16)
            act_ref[:, ff] = act
            x2 = x2 + _dot(act, w2[ff, :])
        r3 = lax.rsqrt(jnp.mean(x2 * x2, axis=-1, keepdims=True) + EPS)
        xh3 = x2 * r3
        err = xh3 * gf_ref[...] - tg_ref[...]
        loss_ref[...] += 0.5 * jnp.sum(jnp.mean(err * err, axis=-1, keepdims=True))
        dy = err * (1.0 / D)
        dgf_ref[...] += jnp.sum(dy * xh3, axis=0, keepdims=True)
        dx2 = _rms_bwd(dy, xh3, r3, gf_ref[...])
        dx2_16 = dx2.astype(BF16)
        dx2_ref[...] = dx2_16
        dh2 = jnp.zeros((TM_FFN, D), F32)
        for c in range(n_chunks):
            ff = slice(c * FF_CHUNK, (c + 1) * FF_CHUNK)
            dpre = (_dot_nt(dx2_16, w2[ff, :]) * (2.0 * relu_s[:, ff])).astype(BF16)
            dpre_ref[:, ff] = dpre
            dh2 = dh2 + _dot(dpre, w1[ff, :])
        dg2_ref[...] += jnp.sum(dh2 * xh2, axis=0, keepdims=True)
        dx1_ref[...] = dx2 + _rms_bwd(dh2, xh2, r2, g2_ref[...])

    tok = lambda width: pl.BlockSpec((TM_FFN, width), lambda i: (i, 0))
    vec = pl.BlockSpec((1, D), lambda i: (0, 0))
    hbm = pl.BlockSpec(memory_space=pl.ANY)
    return pl.pallas_call(
        body, name="ffn_loss",
        out_shape=(jax.ShapeDtypeStruct((8, 128), F32), jax.ShapeDtypeStruct((T, D), F32),
                   jax.ShapeDtypeStruct((T, D), BF16), jax.ShapeDtypeStruct((T, D_FF), BF16),
                   jax.ShapeDtypeStruct((T, D_FF), BF16), jax.ShapeDtypeStruct((T, D), BF16),
                   jax.ShapeDtypeStruct((1, D), F32), jax.ShapeDtypeStruct((1, D), F32)),
        grid=(T // TM_FFN,),
        in_specs=[tok(D), tok(D), vec, vec, hbm, hbm],
        out_specs=(pl.BlockSpec((8, 128), lambda i: (0, 0)), tok(D), tok(D), tok(D_FF), tok(D_FF), tok(D),
                   vec, vec),
        scratch_shapes=[pltpu.VMEM((D_FF, D), BF16), pltpu.VMEM((D_FF, D), BF16),
                        pltpu.VMEM((TM_FFN, D_FF), F32)],
        compiler_params=_params(dimension_semantics=("arbitrary",)),
    )(x1, target, g2, gf, w_ff1_t, w_ff2)


def _local_step(x, target, p, late_weights, reduce_early):
    bias = _rpb_rows(p["rpb"])
    pairs = lambda w: w.reshape(2, N_CG, CG, REC_BLOCK)
    w_a, w_i = pairs(p["w_rg_a"]), pairs(p["w_rg_i"])
    rec_params = (p["conv_w"], p["conv_b"], w_a, p["b_rg_a"], w_i, p["b_rg_i"], p["lru_lambda"])

    qkv, uy, gg, h = _in_proj(x, p["ln1_g"], p["w_in_t"], p["b_in"])
    att = _att_fwd(qkv, bias)
    hf, hb, yrec = _rec_fwd(uy, *rec_params)
    p = {**p, **late_weights(yrec)}
    x1, mixed = _mix_fwd(x, att, yrec, gg, p["w_att_o_t"], p["w_rec_o"], p["w_out"])
    loss8, dx1, h2, act, dpre, dx2, g_ln2, g_lnf = _ffn_loss(
        x1, target, p["ln2_g"], p["lnf_g"], p["w_ff1_t"], p["w_ff2"])

    dgg, dya, dyr, datt, dyrp = _mix_bwd(dx1, att, yrec, gg, p["w_att_o_t"], p["w_rec_o"], p["w_out"])
    duy, g_cw, g_cb, g_wa, g_ba, g_wi, g_bi, g_lam = _rec_bwd(uy, hf, hb, dyrp, *rec_params)
    blocks = lambda g: g.reshape(2, N_REC_BLOCKS, REC_BLOCK, REC_BLOCK)
    grads = {
        "w_att_o_t": _matmul(dya, att, "tn", BF16, "g_w_att_o"),
        "conv_w": g_cw, "conv_b": g_cb, "w_rg_a": blocks(g_wa), "b_rg_a": g_ba,
        "w_rg_i": blocks(g_wi), "b_rg_i": g_bi, "lru_lambda": g_lam,
        "w_rec_o": _matmul(yrec, dyr, "tn", BF16, "g_w_rec_o"),
        "w_out": _matmul(mixed, dx1, "tn", BF16, "g_w_out"),
        "ln2_g": g_ln2,
        "w_ff1_t": _matmul(dpre, h2, "tn", BF16, "g_w_ff1"),
        "w_ff2": _matmul(act, dx2, "tn", BF16, "g_w_ff2"),
        "lnf_g": g_lnf,
    }
    dqkv, gbias = _att_bwd(qkv, bias, datt, reduce_early(grads))
    dz = (dqkv, duy, dgg)
    grad_x, g_ln1 = _dh_norm1_bwd(dz, p["w_in_t"], x, p["ln1_g"], dx1)
    g_w_in_t, g_b_in = _grad_w_in(dz, h)
    grads.update(ln1_g=g_ln1, w_in_t=g_w_in_t, b_in=g_b_in, rpb=_rpb_fold(gbias))
    return loss8[0:1, 0:1], grad_x, grads


MESH_ID = pl.DeviceIdType.MESH
ANY = pl.BlockSpec(memory_space=pl.ANY)

CHAN_BLOCK_ROWS = 32
GATE_ROWS = 2 * 2 * N_REC_BLOCKS * REC_BLOCK * REC_BLOCK // (N_DEV * D)
SECTIONS = (("w_in_t", 704, D), ("w_rec_o", 128, D), ("w_out", 128, D), ("w_ff1_t", 512, D),
            ("w_ff2", 512, D), ("chan", CHAN_BLOCK_ROWS, D), ("w_att_o_t", 128, D_ATT),
            ("gates", GATE_ROWS, D))
N_SEC = len(SECTIONS)
N_CHAN_ROWS = 10
CHAN = (("conv_w", 4), ("b_rg_a", 2), ("b_rg_i", 2), ("lru_lambda", 2))


def _position():
    return lax.axis_index("x"), lax.axis_index("y"), lax.axis_index("c")


def _other_chips(x, y):
    return [(1 - x, y), (x, 1 - y), (1 - x, 1 - y)]


PASS_ON_ID, PAIR_EARLY_ID, PAIR_LATE_ID = 1, 2, 3


def _pair_handshake(x, y, c):
    barrier = pltpu.get_barrier_semaphore()
    pl.semaphore_signal(barrier, inc=1, device_id=(x, y, 1 - c), device_id_type=MESH_ID)
    pl.semaphore_wait(barrier, 1)


def _block_of(ref, dev, rows):
    return ref.at[pl.ds(pl.multiple_of(dev * rows, 16), rows)]


def _all_gather(shards, name):
    ns = len(shards)

    def body(*refs):
        x_refs, out_refs, done_ref = refs[:ns], refs[ns:2 * ns], refs[2 * ns]
        send_sems, recv_sems, local_sems = refs[2 * ns + 1:]
        done_ref[0, 0] = 0.0
        x, y, c = _position()
        me, sibling = (x, y, c), (x, y, 1 - c)
        x_nbr, y_nbr, diagonal = _other_chips(x, y)
        north = c == 1
        relay_from = (jnp.where(north, x_nbr[0], y_nbr[0]), jnp.where(north, x_nbr[1], y_nbr[1]))
        relay_to = (jnp.where(north, y_nbr[0], x_nbr[0]), jnp.where(north, y_nbr[1], x_nbr[1]))

        def rows(s, px, py, pc):
            return _block_of(out_refs[s], 4 * px + 2 * py + pc, shards[s].shape[0])

        def copy(k, s, block, to, from_shard=False):
            return pltpu.make_async_remote_copy(
                src_ref=x_refs[s] if from_shard else rows(s, *block), dst_ref=rows(s, *block),
                send_sem=send_sems.at[k * ns + s], recv_sem=recv_sems.at[k * ns + s],
                device_id=to, device_id_type=MESH_ID)

        sections = range(ns)
        mine = [pltpu.make_async_copy(x_refs[s], rows(s, *me), local_sems.at[s]) for s in sections]
        sent = [copy(k, s, me, to, True) for k, to in enumerate((sibling, (*x_nbr, c), (*y_nbr, c)))
                for s in sections]
        for cp in mine + sent:
            cp.start()
        for s in sections:
            copy(1, s, (*x_nbr, c), me).wait_recv()
            copy(2, s, (*y_nbr, c), me).wait_recv()
            sent += [copy(3, s, (*relay_from, c), (*relay_to, c)),
                     copy(4, s, (*x_nbr, c), sibling), copy(5, s, (*y_nbr, c), sibling)]
            for cp in sent[-3:]:
                cp.start()
        for s in sections:
            copy(3, s, (*diagonal, c), me).wait_recv()
            sent.append(copy(6, s, (*diagonal, c), sibling))
            sent[-1].start()
        for s in sections:
            copy(0, s, sibling, me).wait_recv()
            for k, chip in ((4, x_nbr), (5, y_nbr), (6, diagonal)):
                copy(k, s, (*chip, 1 - c), me).wait_recv()
        for cp in sent:
            cp.wait_send()
        for cp in mine:
            cp.wait()

    return pl.pallas_call(
        body, name=name,
        out_shape=tuple(jax.ShapeDtypeStruct((N_DEV * s.shape[0], s.shape[1]), s.dtype) for s in shards)
        + (jax.ShapeDtypeStruct((1, 1), F32),),
        in_specs=[ANY] * ns,
        out_specs=(ANY,) * ns + (pl.BlockSpec(memory_space=pltpu.SMEM),),
        scratch_shapes=[pltpu.SemaphoreType.DMA((7 * ns,)), pltpu.SemaphoreType.DMA((7 * ns,)),
                        pltpu.SemaphoreType.DMA((ns,))],
    )(*shards)


HBM = pl.BlockSpec(memory_space=pltpu.HBM)
SEM = pl.BlockSpec(memory_space=pltpu.SEMAPHORE)
EFFECT = pltpu.SideEffectType.DATAFLOW_SIDE_EFFECTING


def _in_hbm(a):
    return pltpu.with_memory_space_constraint(a, pltpu.HBM)


def _first_hop_copies(shards, x_refs, zones, send_sems, recv_sems):
    ns = len(shards)
    x, y, c = _position()
    targets = [(x, y, 1 - c)] + [(cx, cy, c) for cx, cy in _other_chips(x, y)]
    return [pltpu.make_async_remote_copy(
        src_ref=x_refs[s], dst_ref=_block_of(zones[s], 4 * x + 2 * y + c, shards[s].shape[0]),
        send_sem=send_sems.at[k * ns + s], recv_sem=recv_sems.at[k * ns + s],
        device_id=to, device_id_type=MESH_ID)
        for k, to in enumerate(targets) for s in range(ns)]


def _after_all(arrays, name):
    def body(*refs):
        refs[-1][...] = jnp.zeros_like(refs[-1])

    return pl.pallas_call(
        body, name=name,
        out_shape=jax.ShapeDtypeStruct((8, LANES), F32),
        in_specs=[pl.BlockSpec(memory_space=pl.ANY)] * len(arrays),
        out_specs=pl.BlockSpec(memory_space=pltpu.VMEM),
    )(*arrays)


def _own_blocks_placed(shards, after):
    ns = len(shards)
    x, y, c = _position()
    me = jnp.reshape(4 * x + 2 * y + c, (1,)).astype(jnp.int32)
    shards = [*shards[:-1], shards[-1] + after.astype(shards[-1].dtype)]

    def body(me_ref, *refs):
        for s in range(ns):
            refs[ns + s][...] = refs[s][...]

    return pl.pallas_call(
        body, name="own_blocks_placed",
        out_shape=tuple(jax.ShapeDtypeStruct((N_DEV * s.shape[0], s.shape[1]), s.dtype) for s in shards),
        grid_spec=pltpu.PrefetchScalarGridSpec(
            num_scalar_prefetch=1, grid=(1,),
            in_specs=[pl.BlockSpec(s.shape, lambda i, me: (0, 0)) for s in shards],
            out_specs=tuple(pl.BlockSpec(s.shape, lambda i, me: (me[0], 0)) for s in shards)),
        compiler_params=_params(dimension_semantics=("arbitrary",)),
    )(me, *shards)


def _gather_start(shards, after, name):
    ns = len(shards)
    zones = _own_blocks_placed(shards, after)

    def body(*refs):
        for cp in _first_hop_copies(shards, refs[:ns], refs[ns:2 * ns], refs[2 * ns], refs[2 * ns + 1]):
            cp.start()
        refs[-1][...] = jnp.zeros_like(refs[-1])

    out = pl.pallas_call(
        body, name=name,
        out_shape=(pltpu.SemaphoreType.DMA((4 * ns,)), pltpu.SemaphoreType.DMA((4 * ns,)),
                   *[pltpu.HBM(a.shape, a.dtype) for a in (*shards, *zones)],
                   jax.ShapeDtypeStruct((8, LANES), F32)),
        in_specs=[HBM] * (2 * ns),
        out_specs=(SEM, SEM, *[HBM] * (2 * ns), pl.BlockSpec(memory_space=pltpu.VMEM)),
        input_output_aliases={i: 2 + i for i in range(2 * ns)},
        compiler_params=pltpu.CompilerParams(has_side_effects=EFFECT),
    )(*[_in_hbm(a) for a in shards], *[_in_hbm(a) for a in zones])
    return out[0], out[1], out[2:2 + ns], out[2 + ns:2 + 2 * ns], out[-1]


def _gather_wait(send_sems, recv_sems, shards, zones, after, name):
    ns = len(shards)

    def body(*refs):
        for cp in _first_hop_copies(shards, refs[:ns], refs[ns:2 * ns], refs[2 * ns], refs[2 * ns + 1]):
            cp.wait_send()
            cp.wait_recv()

    out = pl.pallas_call(
        body, name=name,
        out_shape=tuple(pltpu.HBM(a.shape, a.dtype) for a in (*shards, *zones)),
        in_specs=[HBM] * (2 * ns) + [SEM, SEM, ANY],
        out_specs=(HBM,) * (2 * ns),
        input_output_aliases={i: i for i in range(2 * ns)},
        compiler_params=pltpu.CompilerParams(has_side_effects=EFFECT),
    )(*shards, *zones, send_sems, recv_sems, after)
    return out[ns:]


def _gather_pass_on(rows, zones, name):
    ns = len(zones)

    def body(*refs):
        in_refs, out_refs = refs[:ns], refs[ns:2 * ns]
        send_sems, recv_sems = refs[2 * ns:]
        x, y, c = _position()
        _pair_handshake(x, y, c)
        copies = [pltpu.make_async_remote_copy(
            src_ref=_block_of(in_refs[s], 4 * cx + 2 * cy + c, rows[s]),
            dst_ref=_block_of(out_refs[s], 4 * cx + 2 * cy + c, rows[s]),
            send_sem=send_sems.at[j * ns + s], recv_sem=recv_sems.at[j * ns + s],
            device_id=(x, y, 1 - c), device_id_type=MESH_ID)
            for j, (cx, cy) in enumerate(_other_chips(x, y)) for s in range(ns)]
        for cp in copies:
            cp.start()
        for cp in copies:
            cp.wait_recv()
        for cp in copies:
            cp.wait_send()

    return pl.pallas_call(
        body, name=name,
        out_shape=tuple(jax.ShapeDtypeStruct(z.shape, z.dtype) for z in zones),
        in_specs=[ANY] * ns, out_specs=(ANY,) * ns,
        input_output_aliases={i: i for i in range(ns)},
        scratch_shapes=[pltpu.SemaphoreType.DMA((3 * ns,)), pltpu.SemaphoreType.DMA((3 * ns,))],
        compiler_params=pltpu.CompilerParams(collective_id=PASS_ON_ID),
    )(*zones)


def _pair_copies(sections, g_refs, land, send_sems, recv_sems):
    ns = len(sections)
    x, y, c = _position()
    return [pltpu.make_async_remote_copy(
        src_ref=_block_of(g_refs[s], 2 * k + 1 - c, rows), dst_ref=land[s].at[k],
        send_sem=send_sems.at[k * ns + s], recv_sem=recv_sems.at[k * ns + s],
        device_id=(x, y, 1 - c), device_id_type=MESH_ID)
        for k in range(N_CHIPS) for s, (_, rows, _) in enumerate(sections)]


def _pair_exchange_start(sections, grads, barrier_id, name):
    ns = len(sections)

    def body(*refs):
        _pair_handshake(*_position())
        for cp in _pair_copies(sections, refs[:ns], refs[ns:2 * ns], refs[2 * ns], refs[2 * ns + 1]):
            cp.start()
        refs[-1][...] = jnp.zeros_like(refs[-1])

    zones = [lax.empty((N_CHIPS, rows, cols), BF16) for _, rows, cols in sections]
    n = N_CHIPS * ns
    out = pl.pallas_call(
        body, name=name,
        out_shape=(pltpu.SemaphoreType.DMA((n,)), pltpu.SemaphoreType.DMA((n,)),
                   *[pltpu.HBM(a.shape, a.dtype) for a in (*grads, *zones)],
                   jax.ShapeDtypeStruct((8, LANES), F32)),
        in_specs=[HBM] * (2 * ns),
        out_specs=(SEM, SEM, *[HBM] * (2 * ns), pl.BlockSpec(memory_space=pltpu.VMEM)),
        input_output_aliases={i: 2 + i for i in range(2 * ns)},
        compiler_params=pltpu.CompilerParams(has_side_effects=EFFECT, collective_id=barrier_id),
    )(*[_in_hbm(a) for a in grads], *[_in_hbm(a) for a in zones])
    return out[0], out[1], out[2:2 + ns], out[2 + ns:2 + 2 * ns], out[-1]


def _pair_exchange_wait(sections, send_sems, recv_sems, grads, zones, after, name):
    ns = len(sections)

    def body(*refs):
        for cp in _pair_copies(sections, refs[:ns], refs[ns:2 * ns], refs[2 * ns], refs[2 * ns + 1]):
            cp.wait_send()
            cp.wait_recv()

    out = pl.pallas_call(
        body, name=name,
        out_shape=tuple(pltpu.HBM(a.shape, a.dtype) for a in (*grads, *zones)),
        in_specs=[HBM] * (2 * ns) + [SEM, SEM, ANY],
        out_specs=(HBM,) * (2 * ns),
        input_output_aliases={i: i for i in range(2 * ns)},
        compiler_params=pltpu.CompilerParams(has_side_effects=EFFECT),
    )(*grads, *zones, send_sems, recv_sems, after)
    return out[:ns], out[ns:]


def _pair_add(sections, grads, got, core, name):
    ns = len(sections)

    def body(core_ref, *refs):
        g_refs, got_refs, p_refs = refs[:ns], refs[ns:2 * ns], refs[2 * ns:]
        for s in range(ns):
            p_refs[s][0] = (g_refs[s][...].astype(F32) + got_refs[s][0].astype(F32)).astype(BF16)

    slot = [pl.BlockSpec((1, rows, cols), lambda k, c: (k, 0, 0)) for _, rows, cols in sections]
    return pl.pallas_call(
        body, name=name,
        out_shape=tuple(jax.ShapeDtypeStruct((N_CHIPS, rows, cols), BF16) for _, rows, cols in sections),
        grid_spec=pltpu.PrefetchScalarGridSpec(
            num_scalar_prefetch=1, grid=(N_CHIPS,),
            in_specs=[pl.BlockSpec((rows, cols), lambda k, c: (2 * k + c[0], 0)) for _, rows, cols in sections]
            + slot,
            out_specs=tuple(slot)),
        compiler_params=_params(dimension_semantics=("parallel",)),
    )(core, *grads, *got)


def _chip_copies(sections, p_refs, land, send_sems, recv_sems):
    ns = len(sections)
    x, y, c = _position()
    return [pltpu.make_async_remote_copy(
        src_ref=p_refs[s].at[2 * cx + cy], dst_ref=land[s].at[j],
        send_sem=send_sems.at[j * ns + s], recv_sem=recv_sems.at[j * ns + s],
        device_id=(cx, cy, c), device_id_type=MESH_ID)
        for j, (cx, cy) in enumerate(_other_chips(x, y)) for s in range(ns)]


def _chip_exchange(sections, parts, name):
    ns = len(sections)

    def body(*refs):
        copies = _chip_copies(sections, refs[:ns], refs[ns:2 * ns], *refs[2 * ns:])
        for cp in copies:
            cp.start()
        for cp in copies:
            cp.wait_recv()
        for cp in copies:
            cp.wait_send()

    n = 3 * ns
    return pl.pallas_call(
        body, name=name,
        out_shape=tuple(jax.ShapeDtypeStruct((3, rows, cols), BF16) for _, rows, cols in sections),
        in_specs=[ANY] * ns, out_specs=(ANY,) * ns,
        scratch_shapes=[pltpu.SemaphoreType.DMA((n,)), pltpu.SemaphoreType.DMA((n,))],
    )(*parts)


def _chip_exchange_start(sections, parts, name):
    ns = len(sections)

    def body(*refs):
        p_refs, land = refs[:ns], refs[ns:2 * ns]
        send_sems, recv_sems = refs[2 * ns], refs[2 * ns + 1]
        token = refs[-1]
        for cp in _chip_copies(sections, p_refs, land, send_sems, recv_sems):
            cp.start()
        token[...] = jnp.zeros_like(token)

    zones = [lax.empty((3, rows, cols), BF16) for _, rows, cols in sections]
    out = pl.pallas_call(
        body, name=name,
        out_shape=(pltpu.SemaphoreType.DMA((3 * ns,)), pltpu.SemaphoreType.DMA((3 * ns,)),
                   *[pltpu.HBM(a.shape, a.dtype) for a in parts], *[pltpu.HBM(a.shape, a.dtype) for a in zones],
                   jax.ShapeDtypeStruct((8, LANES), F32)),
        in_specs=[HBM] * (2 * ns),
        out_specs=(SEM, SEM, *[HBM] * (2 * ns), pl.BlockSpec(memory_space=pltpu.VMEM)),
        input_output_aliases={i: 2 + i for i in range(2 * ns)},
        compiler_params=pltpu.CompilerParams(has_side_effects=EFFECT),
    )(*[_in_hbm(a) for a in parts], *[_in_hbm(a) for a in zones])
    return out[0], out[1], out[2:2 + ns], out[2 + ns:2 + 2 * ns], out[-1]


def _chip_exchange_wait(sections, send_sems, recv_sems, parts, zones, after, name):
    ns = len(sections)

    def body(*refs):
        p_refs, land = refs[:ns], refs[ns:2 * ns]
        for cp in _chip_copies(sections, p_refs, land, refs[2 * ns], refs[2 * ns + 1]):
            cp.wait_send()
            cp.wait_recv()

    out = pl.pallas_call(
        body, name=name,
        out_shape=tuple(pltpu.HBM(a.shape, a.dtype) for a in (*parts, *zones)),
        in_specs=[HBM] * (2 * ns) + [SEM, SEM, ANY],
        out_specs=(HBM,) * (2 * ns),
        input_output_aliases={i: i for i in range(2 * ns)},
        compiler_params=pltpu.CompilerParams(has_side_effects=EFFECT),
    )(*parts, *zones, send_sems, recv_sems, after)
    return out[:ns], out[ns:]


def _grad_finish(sections, parts, far, chip, name):
    ns = len(sections)

    def body(chip_ref, *refs):
        p_refs, b_refs, g_refs = refs[:ns], refs[ns:2 * ns], refs[2 * ns:]
        for s in range(ns):
            g = p_refs[s][0].astype(F32)
            for j in range(3):
                g = g + b_refs[s][j].astype(F32)
            g_refs[s][...] = g

    half = [(rows // 2, cols) for _, rows, cols in sections]
    return pl.pallas_call(
        body, name=name,
        out_shape=tuple(jax.ShapeDtypeStruct((rows, cols), F32) for _, rows, cols in sections),
        grid_spec=pltpu.PrefetchScalarGridSpec(
            num_scalar_prefetch=1, grid=(2,),
            in_specs=[pl.BlockSpec((1, r, c), lambda i, chip: (chip[0], i, 0)) for r, c in half]
            + [pl.BlockSpec((3, r, c), lambda i, chip: (0, i, 0)) for r, c in half],
            out_specs=tuple(pl.BlockSpec((r, c), lambda i, chip: (i, 0)) for r, c in half)),
        compiler_params=_params(dimension_semantics=("parallel",)),
    )(chip, *parts, *far)


def _sum_devices(parts, rows, name):
    cols = parts.shape[1]
    tr = rows // 2

    def body(*refs):
        s = refs[0][...].astype(F32)
        for d in range(1, N_DEV):
            s = s + refs[d][...].astype(F32)
        refs[N_DEV][...] = s

    return pl.pallas_call(
        body, name=name,
        out_shape=jax.ShapeDtypeStruct((rows, cols), F32),
        grid=(2,),
        in_specs=[pl.BlockSpec((tr, cols), lambda i, d=d: (2 * d + i, 0)) for d in range(N_DEV)],
        out_specs=pl.BlockSpec((tr, cols), lambda i: (i, 0)),
        compiler_params=_params(dimension_semantics=("parallel",)),
    )(*([parts] * N_DEV))


def _adamw_step(w_ref, g_ref, m_ref, v_ref, d_ref, nm_ref, nv_ref):
    c1 = 1.0 / (1.0 - ADAM_B1 ** ADAM_STEP)
    c2 = 1.0 / (1.0 - ADAM_B2 ** ADAM_STEP)
    gv = g_ref[...]
    nm = ADAM_B1 * m_ref[...] + (1.0 - ADAM_B1) * gv
    nv = ADAM_B2 * v_ref[...] + (1.0 - ADAM_B2) * (gv * gv)
    nm_ref[...] = nm
    nv_ref[...] = nv
    d_ref[...] = (-ADAM_LR) * ((nm * c1) / (jnp.sqrt(nv * c2) + ADAM_EPS) + ADAM_WD * w_ref[...])


def _adamw_small(params, name):
    n = len(params)

    def body(*refs):
        for k in range(n):
            _adamw_step(*refs[4 * k:4 * k + 4], *refs[4 * n + 3 * k:4 * n + 3 * k + 3])

    out = pl.pallas_call(
        body, name=name,
        out_shape=tuple(jax.ShapeDtypeStruct(p[0].shape, F32) for p in params for _ in range(3)),
    )(*[a for p in params for a in p])
    return [out[3 * k:3 * k + 3] for k in range(n)]


def _adamw(w, g, m, v, name):
    rows, cols = w.shape
    tr = rows
    while tr * cols * 4 > (1 << 20) and tr % 16 == 0:
        tr //= 2

    def body(*refs):
        _adamw_step(*refs)

    spec = pl.BlockSpec((tr, cols), lambda i: (i, 0))
    shape = jax.ShapeDtypeStruct((rows, cols), F32)
    return pl.pallas_call(
        body, name=name,
        out_shape=(shape, shape, shape),
        grid=(rows // tr,),
        in_specs=[spec] * 4, out_specs=(spec,) * 3,
        compiler_params=_params(dimension_semantics=("parallel",)),
    )(w, g, m, v)


NAMES = ("ln1_g", "w_in", "b_in", "rpb", "w_att_o", "conv_w", "conv_b", "w_rg_a", "b_rg_a", "w_rg_i",
         "b_rg_i", "lru_lambda", "w_rec_o", "w_out", "ln2_g", "w_ff1", "w_ff2", "lnf_g")
TRANSPOSED = {"w_in": "w_in_t", "w_att_o": "w_att_o_t", "w_ff1": "w_ff1_t"}
ROW_SHARDED = ("w_rec_o", "w_out", "w_ff2")
REPLICATED = (("ln1_g", (1, D)), ("b_in", (1, D_IN)), ("rpb", (N_HEADS * N_RPB_R, N_RPB_C)),
              ("conv_b", (1, D_REC)), ("w_rg_a", (2 * N_REC_BLOCKS * REC_BLOCK, REC_BLOCK)),
              ("w_rg_i", (2 * N_REC_BLOCKS * REC_BLOCK, REC_BLOCK)), ("ln2_g", (1, D)), ("lnf_g", (1, D)))
GATE_BLOCKS = ("w_rg_a", "w_rg_i")
SMALL_ROWS = 112


def _chan_bits(vectors):
    chan = jnp.concatenate(vectors, axis=0)
    bits = lax.bitcast_convert_type(chan, BF16).reshape(-1)
    return jnp.pad(bits, (0, CHAN_BLOCK_ROWS * D - bits.shape[0])).reshape(CHAN_BLOCK_ROWS, D)


def _chan_from_bits(gathered):
    bits = gathered.reshape(N_DEV, CHAN_BLOCK_ROWS * D)[:, :2 * N_CHAN_ROWS * LANES]
    chan = lax.bitcast_convert_type(bits.reshape(N_DEV, N_CHAN_ROWS, LANES, 2), F32)
    return chan.transpose(1, 0, 2).reshape(N_CHAN_ROWS, D)


def kernel(x, ln1_g, w_in, b_in, rpb, w_att_o, conv_w, conv_b, w_rg_a, b_rg_a, w_rg_i, b_rg_i, lru_lambda, w_rec_o, w_out, ln2_g, w_ff1, w_ff2, lnf_g, loss_target, m_ln1_g, m_w_in, m_b_in, m_rpb, m_w_att_o, m_conv_w, m_conv_b, m_w_rg_a, m_b_rg_a, m_w_rg_i, m_b_rg_i, m_lru_lambda, m_w_rec_o, m_w_out, m_ln2_g, m_w_ff1, m_w_ff2, m_lnf_g, v_ln1_g, v_w_in, v_b_in, v_rpb, v_w_att_o, v_conv_w, v_conv_b, v_w_rg_a, v_b_rg_a, v_w_rg_i, v_b_rg_i, v_lru_lambda, v_w_rec_o, v_w_out, v_ln2_g, v_w_ff1, v_w_ff2, v_lnf_g):
    w = dict(zip(NAMES, (ln1_g, w_in, b_in, rpb, w_att_o, conv_w, conv_b, w_rg_a, b_rg_a, w_rg_i,
                         b_rg_i, lru_lambda, w_rec_o, w_out, ln2_g, w_ff1, w_ff2, lnf_g)))
    m = dict(zip(NAMES, (m_ln1_g, m_w_in, m_b_in, m_rpb, m_w_att_o, m_conv_w, m_conv_b, m_w_rg_a,
                         m_b_rg_a, m_w_rg_i, m_b_rg_i, m_lru_lambda, m_w_rec_o, m_w_out, m_ln2_g,
                         m_w_ff1, m_w_ff2, m_lnf_g)))
    v = dict(zip(NAMES, (v_ln1_g, v_w_in, v_b_in, v_rpb, v_w_att_o, v_conv_w, v_conv_b, v_w_rg_a,
                         v_b_rg_a, v_w_rg_i, v_b_rg_i, v_lru_lambda, v_w_rec_o, v_w_out, v_ln2_g,
                         v_w_ff1, v_w_ff2, v_lnf_g)))
    xi, yi, ci = _position()

    shard = {t: w[n][0].T.astype(BF16) for n, t in TRANSPOSED.items()}
    shard.update({n: w[n][0].astype(BF16) for n in ROW_SHARDED})
    shard["chan"] = _chan_bits([w[n][0] for n, _ in CHAN])
    first, later = ("w_in_t", "chan"), ("w_rec_o", "w_out", "w_att_o_t", "w_ff1_t", "w_ff2")
    *gathered, done = _all_gather([shard[n] for n in first], "weight_all_gather")
    p = dict(zip(first, gathered))
    send_sems, recv_sems, sent, zones, token = _gather_start([shard[n] for n in later], done,
                                                             "weight_gather_start")

    def late_weights(after):
        landed = _gather_wait(send_sems, recv_sems, sent, zones, after, "weight_gather_wait")
        return dict(zip(later, _gather_pass_on([shard[n].shape[0] for n in later], landed,
                                               "weight_gather_pass_on")))

    chan = _chan_from_bits(p.pop("chan"))
    r0 = 0
    for n, rows in CHAN:
        p[n] = chan[r0:r0 + rows]
        r0 += rows
    p.update(ln1_g=w["ln1_g"], b_in=w["b_in"] + token[0, 0], rpb=w["rpb"][0], conv_b=w["conv_b"],
             w_rg_a=w["w_rg_a"][0], w_rg_i=w["w_rg_i"][0], ln2_g=w["ln2_g"],
             lnf_g=w["lnf_g"].reshape(1, D))

    core = jnp.reshape(ci, (1,)).astype(jnp.int32)
    chip = jnp.reshape(2 * xi + yi, (1,)).astype(jnp.int32)
    early_sections, late_sections = SECTIONS[1:], SECTIONS[:1]
    in_flight = {}

    def pair_sum_and_send(group, sections, after):
        send_sems, recv_sems, sect, zones, _ = in_flight["pair_" + group]
        sect, got = _pair_exchange_wait(sections, send_sems, recv_sems, sect, zones, after,
                                        "grad_pair_exchange_wait_" + group)
        parts = _pair_add(sections, sect, got, core, "grad_pair_add_" + group)
        in_flight[group] = _chip_exchange_start(sections, parts, "grad_chip_exchange_start_" + group)
        return in_flight[group][-1]

    def reduce_early(grads):
        chan_g = jnp.concatenate([grads[n] for n, _ in CHAN], axis=0)
        chan_g = chan_g.reshape(N_CHAN_ROWS, N_DEV, LANES).transpose(1, 0, 2).astype(BF16)
        chan_g = jnp.pad(chan_g.reshape(N_DEV, -1), ((0, 0), (0, CHAN_BLOCK_ROWS * D - N_CHAN_ROWS * LANES)))
        grads["chan"] = chan_g.reshape(N_DEV * CHAN_BLOCK_ROWS, D)
        grads["gates"] = jnp.concatenate([grads[n].reshape(-1, D) for n in GATE_BLOCKS], axis=0).astype(BF16)
        in_flight["pair_early"] = _pair_exchange_start(
            early_sections, [grads[n] for n, _, _ in early_sections], PAIR_EARLY_ID,
            "grad_pair_exchange_start_early")
        return pair_sum_and_send("early", early_sections, in_flight["pair_early"][-1])[0, 0]

    loss_part, grad_x, grads = _local_step(x[0], loss_target[0], p, late_weights, reduce_early)
    in_flight["pair_late"] = _pair_exchange_start(
        late_sections, [grads[n] for n, _, _ in late_sections], PAIR_LATE_ID, "grad_pair_exchange_start_late")

    def finish(group, sections, after, name):
        send_sems, recv_sems, parts, zones, _ = in_flight[group]
        parts, far = _chip_exchange_wait(sections, send_sems, recv_sems, parts, zones, after,
                                         "grad_chip_exchange_wait_" + name)
        return dict(zip((n for n, _, _ in sections),
                        _grad_finish(sections, parts, far, chip, "grad_finish_" + name)))

    summed = finish("early", early_sections, in_flight["pair_late"][-1], "early")
    started_late = pair_sum_and_send("late", late_sections, summed["gates"])

    flat = jnp.concatenate([grads[n].reshape(-1) for n, _ in REPLICATED if n not in GATE_BLOCKS]
                           + [loss_part.reshape(-1) + started_late[0, 0]])
    n_small = flat.shape[0]
    flat = jnp.pad(flat, (0, SMALL_ROWS * LANES - n_small)).reshape(SMALL_ROWS, LANES)
    small_parts, gate_sum, _ = _all_gather([flat, summed["gates"]], "small_grad_all_gather")
    small = _sum_devices(small_parts, SMALL_ROWS, "small_grad_sum").reshape(-1)
    loss = small[n_small - 1]

    g, delta, new_m, new_v = {}, {}, {}, {}

    def update(n, g2, shape2):
        d2, m2, v2 = _adamw(w[n].reshape(shape2), g2, m[n].reshape(shape2), v[n].reshape(shape2),
                            "adamw_" + n)
        g[n], delta[n], new_m[n], new_v[n] = (a.reshape(w[n].shape) for a in (g2, d2, m2, v2))

    small_params = []
    o = 0
    for n, shape2 in REPLICATED:
        if n in GATE_BLOCKS:
            k, rows = GATE_BLOCKS.index(n), gate_sum.shape[0] // len(GATE_BLOCKS)
            update(n, gate_sum[k * rows:(k + 1) * rows].reshape(shape2), shape2)
        else:
            size = shape2[0] * shape2[1]
            small_params.append((n, small[o:o + size].reshape(shape2), shape2))
            o += size
    chan_back = summed["chan"].reshape(-1)[:N_CHAN_ROWS * LANES].reshape(N_CHAN_ROWS, LANES)
    r0 = 0
    for n, rows in CHAN:
        small_params.append((n, chan_back[r0:r0 + rows], (rows, LANES)))
        r0 += rows
    results = _adamw_small([(w[n].reshape(s2), g2, m[n].reshape(s2), v[n].reshape(s2))
                            for n, g2, s2 in small_params], "adamw_vectors")
    for (n, g2, _), (d2, m2, v2) in zip(small_params, results):
        g[n], delta[n], new_m[n], new_v[n] = (a.reshape(w[n].shape) for a in (g2, d2, m2, v2))

    for n in ROW_SHARDED:
        update(n, summed[n], summed[n].shape)
    for n, t in TRANSPOSED.items():
        if t in summed:
            update(n, summed[t].T, summed[t].shape[::-1])
    summed = finish("late", late_sections, _after_all(list(delta.values()), "updates_done"), "late")
    update("w_in", summed["w_in_t"].T, summed["w_in_t"].shape[::-1])

    return (loss, grad_x[None], *[g[n] for n in NAMES], *[delta[n] for n in NAMES],
            *[new_m[n] for n in NAMES], *[new_v[n] for n in NAMES])
```

```python
import math

import numpy as np
import jax
import jax.numpy as jnp
from jax import lax
from jax.experimental import pallas as pl
from jax.experimental.pallas import tpu as pltpu

F32 = jnp.float32
BF16 = jnp.bfloat16

T = 2048
D = 1024
D_ATT = 512
D_REC = 1024
D_FF = 4096
D_IN = 5632
N_HEADS = 8
DH = 64
GRID_W = 64
ROWS = T // GRID_W
WIN_H = 8
WIN_W = 16
KWIN = WIN_H * GRID_W
N_RPB_R = 2 * WIN_H - 1
N_RPB_C = 2 * WIN_W - 1
N_REC_BLOCKS = 16
REC_BLOCK = 64
CG = 128
N_CG = D_REC // CG
LRU_C = 8.0
EPS = 1e-6
N_DEV = 8
N_CHIPS = 4
LANES = 128

ADAM_LR = 0.001
ADAM_B1 = 0.9
ADAM_B2 = 0.999
ADAM_EPS = 1e-08
ADAM_WD = 0.01
ADAM_STEP = 10

MESH_AXES = ("x", "y", "c")
VMEM_LIMIT = 56 * 1024 * 1024

TILE = 512
DZ_ARRAYS = ((0, 3, 1), (3, 4, 2), (7, 4, 2))
N_DZ_TILES = D_IN // TILE


def _params(**kw):
    return pltpu.CompilerParams(vmem_limit_bytes=VMEM_LIMIT, **kw)


HG = 4
HQ = HG * GRID_W
HC = HG * DH


def _att_tables():
    rq = np.arange(GRID_W)
    kc = np.arange(KWIN) % GRID_W
    win_start = np.clip(rq - WIN_W // 2, 0, GRID_W - WIN_W)
    valid = (kc[None, :] >= win_start[:, None]) & (kc[None, :] < win_start[:, None] + WIN_W)
    same_head = (np.arange(HQ)[:, None] // GRID_W) == (np.arange(HC)[None, :] // DH)
    return valid.astype(np.float32), same_head.astype(np.float32)


def _pair_mask():
    half = np.arange(2 * DH) // DH
    return (half[:, None] == half[None, :]).astype(np.float32)


def _dup_table():
    return np.concatenate([np.eye(REC_BLOCK, dtype=np.float32)] * 2, axis=1)


def _sigmoid(x):
    return 0.5 * jnp.tanh(0.5 * x) + 0.5


def _softplus(x):
    return jnp.maximum(x, 0.0) + jnp.log(1.0 + jnp.exp(-jnp.abs(x)))


def _one_minus_square(log_a, a):
    x = 2.0 * log_a
    series = -x * (1.0 + x * (0.5 + x * (1.0 / 6.0)))
    return jnp.where(x > -0.02, series, 1.0 - a * a)


_GELU_C = math.sqrt(2.0 / math.pi)


def _gelu_and_grad(x):
    x2 = x * x
    inner = _GELU_C * (x + 0.044715 * x * x2)
    t = jnp.tanh(inner)
    g = 0.5 * x * (1.0 + t)
    dg = 0.5 * (1.0 + t) + 0.5 * x * (1.0 - t * t) * _GELU_C * (1.0 + 3.0 * 0.044715 * x2)
    return g, dg


def _dot(a, b):
    return jnp.dot(a, b, preferred_element_type=F32)


def _dot_nt(a, b):
    return lax.dot_general(a, b, (((1,), (1,)), ((), ())), preferred_element_type=F32)


def _dot_tn(a, b):
    return lax.dot_general(a, b, (((0,), (0,)), ((), ())), preferred_element_type=F32)


def _dot_exact(a, b):
    return jnp.dot(a, b, precision=lax.Precision.HIGHEST, preferred_element_type=F32)


def _shift_rows(x, s):
    n = x.shape[0]
    rows = lax.broadcasted_iota(jnp.int32, x.shape, 0)
    y = pltpu.roll(x, s % n, 0)
    if s > 0:
        return jnp.where(rows >= s, y, 0.0)
    return jnp.where(rows < n + s, y, 0.0)


def _rms_bwd(dh, xh, r, g):
    dxh = dh * g
    return r * (dxh - xh * jnp.mean(dxh * xh, axis=-1, keepdims=True))


def _matmul(a, b, mode, out_dtype, name, tm=512, tn=1024, tk=2048):
    if mode == "nn":
        (m, k), (k2, n) = a.shape, b.shape
    elif mode == "nt":
        (m, k), (n, k2) = a.shape, b.shape
    else:
        (k, m), (k2, n) = a.shape, b.shape
    assert k == k2
    tm, tn, tk = min(tm, m), min(tn, n), min(tk, k)
    assert m % tm == 0 and n % tn == 0 and k % tk == 0
    nk = k // tk
    dot = {"nn": _dot, "nt": _dot_nt, "tn": _dot_tn}[mode]

    def body(a_ref, b_ref, o_ref, acc):
        kk = pl.program_id(2)
        part = dot(a_ref[...].astype(BF16), b_ref[...].astype(BF16))
        if nk == 1:
            o_ref[...] = part.astype(out_dtype)
            return

        @pl.when(kk == 0)
        def _():
            acc[...] = part

        @pl.when(kk > 0)
        def _():
            acc[...] += part

        @pl.when(kk == nk - 1)
        def _():
            o_ref[...] = acc[...].astype(out_dtype)

    if mode == "tn":
        a_spec = pl.BlockSpec((tk, tm), lambda i, j, kk: (kk, i))
    else:
        a_spec = pl.BlockSpec((tm, tk), lambda i, j, kk: (i, kk))
    if mode == "nt":
        b_spec = pl.BlockSpec((tn, tk), lambda i, j, kk: (j, kk))
    else:
        b_spec = pl.BlockSpec((tk, tn), lambda i, j, kk: (kk, j))
    return pl.pallas_call(
        body, name=name,
        out_shape=jax.ShapeDtypeStruct((m, n), out_dtype),
        grid=(m // tm, n // tn, nk),
        in_specs=[a_spec, b_spec],
        out_specs=pl.BlockSpec((tm, tn), lambda i, j, kk: (i, j)),
        scratch_shapes=[pltpu.VMEM((tm, tn) if nk > 1 else (8, LANES), F32)],
        compiler_params=_params(dimension_semantics=("parallel", "parallel", "arbitrary")),
    )(a, b)


def _in_proj(x, g1, w_in_t, b_in):
    tm = 1024

    def body(x_ref, g_ref, w_ref, b_ref, qkv_ref, uy_ref, gg_ref, h_ref, h_scr):
        j = pl.program_id(1)

        @pl.when(j == 0)
        def _():
            xv = x_ref[...]
            r = lax.rsqrt(jnp.mean(xv * xv, axis=-1, keepdims=True) + EPS)
            h = ((xv * r) * g_ref[...]).astype(BF16)
            h_scr[...] = h
            h_ref[...] = h

        z = _dot_nt(h_scr[...], w_ref[...]) + b_ref[...]

        @pl.when(j < 3)
        def _():
            qkv_ref[...] = z.astype(BF16)

        @pl.when((j >= 3) & (j < 7))
        def _():
            uy_ref[...] = z

        @pl.when(j >= 7)
        def _():
            gg_ref[...] = z

    return pl.pallas_call(
        body, name="in_proj",
        out_shape=(jax.ShapeDtypeStruct((T, 3 * D_ATT), BF16),
                   jax.ShapeDtypeStruct((T, 2 * D_REC), F32),
                   jax.ShapeDtypeStruct((T, 2 * D), F32),
                   jax.ShapeDtypeStruct((T, D), BF16)),
        grid=(T // tm, N_DZ_TILES),
        in_specs=[pl.BlockSpec((tm, D), lambda i, j: (i, 0)),
                  pl.BlockSpec((1, D), lambda i, j: (0, 0)),
                  pl.BlockSpec((TILE, D), lambda i, j: (j, 0)),
                  pl.BlockSpec((1, TILE), lambda i, j: (0, j))],
        out_specs=(pl.BlockSpec((tm, TILE), lambda i, j: (i, jnp.minimum(j, 2))),
                   pl.BlockSpec((tm, TILE), lambda i, j: (i, jnp.clip(j - 3, 0, 3))),
                   pl.BlockSpec((tm, TILE), lambda i, j: (i, jnp.clip(j - 7, 0, 3))),
                   pl.BlockSpec((tm, D), lambda i, j: (i, 0))),
        scratch_shapes=[pltpu.VMEM((tm, D), BF16)],
        compiler_params=_params(dimension_semantics=("parallel", "arbitrary")),
    )(x, g1, w_in_t, b_in)


def _dz_specs(rows, tile_of, row_of):
    def spec(off, n, per_plane):
        def index(*ids):
            t = jnp.clip(tile_of(*ids) - off, 0, n - 1)
            return (t // per_plane, row_of(*ids), t % per_plane)
        return pl.BlockSpec((1, rows, TILE), index)
    return [spec(off, n, per) for off, n, per in DZ_ARRAYS]


def _dh_norm1_bwd(dz, w_in_t, x, g1, dx1):
    tm = 1024

    def body(*refs):
        seg_refs = refs[:3]
        w_ref, x_ref, g_ref, dx1_ref, gx_ref, dg_ref, acc = refs[3:]
        i, kk = pl.program_id(0), pl.program_id(1)

        @pl.when(kk == 0)
        def _():
            acc[...] = jnp.zeros_like(acc)

        for s, (off, n, _) in enumerate(DZ_ARRAYS):
            @pl.when((kk >= off) & (kk < off + n))
            def _(s=s):
                acc[...] += _dot(seg_refs[s][0], w_ref[...])

        @pl.when((i == 0) & (kk == 0))
        def _():
            dg_ref[...] = jnp.zeros_like(dg_ref)

        @pl.when(kk == N_DZ_TILES - 1)
        def _():
            xv = x_ref[...]
            r = lax.rsqrt(jnp.mean(xv * xv, axis=-1, keepdims=True) + EPS)
            xh = xv * r
            dh = acc[...]
            dg_ref[...] += jnp.sum(dh * xh, axis=0, keepdims=True)
            gx_ref[...] = dx1_ref[...] + _rms_bwd(dh, xh, r, g_ref[...])

    tok = pl.BlockSpec((tm, D), lambda i, j: (i, 0))
    vec = pl.BlockSpec((1, D), lambda i, j: (0, 0))
    return pl.pallas_call(
        body, name="dh_norm1_bwd",
        out_shape=(jax.ShapeDtypeStruct((T, D), F32), jax.ShapeDtypeStruct((1, D), F32)),
        grid=(T // tm, N_DZ_TILES),
        in_specs=_dz_specs(tm, lambda i, j: j, lambda i, j: i)
        + [pl.BlockSpec((TILE, D), lambda i, j: (j, 0)), tok, vec, tok],
        out_specs=(tok, vec),
        scratch_shapes=[pltpu.VMEM((tm, D), F32)],
        compiler_params=_params(dimension_semantics=("arbitrary", "arbitrary")),
    )(*dz, w_in_t, x, g1, dx1)


def _grad_w_in(dz, h):
    def body(*refs):
        seg_refs = refs[:3]
        h_ref, gw_ref, gb_ref = refs[3:]
        j = pl.program_id(0)

        for s, (off, n, _) in enumerate(DZ_ARRAYS):
            @pl.when((j >= off) & (j < off + n))
            def _(s=s):
                a = seg_refs[s][0]
                gw_ref[...] = _dot_tn(a, h_ref[...]).astype(BF16)
                gb_ref[...] = jnp.sum(a.astype(F32), axis=0, keepdims=True)

    return pl.pallas_call(
        body, name="grad_w_in",
        out_shape=(jax.ShapeDtypeStruct((D_IN, D), BF16), jax.ShapeDtypeStruct((1, D_IN), F32)),
        grid=(N_DZ_TILES,),
        in_specs=_dz_specs(T, lambda j: j, lambda j: 0) + [pl.BlockSpec((T, D), lambda j: (0, 0))],
        out_specs=(pl.BlockSpec((TILE, D), lambda j: (j, 0)), pl.BlockSpec((1, TILE), lambda j: (0, j))),
        compiler_params=_params(dimension_semantics=("parallel",)),
    )(*dz, h)


def _rpb_rows(rpb):
    padded = jnp.pad(rpb, ((0, 0), (0, 0), (0, GRID_W - N_RPB_C)))
    rows = [padded[:, WIN_H - 1 - oi: 2 * WIN_H - 1 - oi].reshape(N_HEADS // HG, HG, KWIN)
            for oi in range(WIN_H)]
    return jnp.stack(rows, axis=0)


SKEW = KWIN - (WIN_W - 1)


MASKED = -1e30


def _bias_tiles(rows_ref, valid, bias_s):
    for oi in range(WIN_H):
        for hh in range(HG):
            row = jnp.broadcast_to(rows_ref[oi, 0, hh:hh + 1, :], (GRID_W, KWIN))
            tile = pltpu.roll(row, SKEW, 1, stride=1, stride_axis=0)
            bias_s[oi, hh * GRID_W:(hh + 1) * GRID_W, :] = jnp.where(valid, tile, MASKED)


def _bias_tile_grads(gb_s, flip, out_ref):
    for oi in range(WIN_H):
        for hh in range(HG):
            g = _dot_exact(flip, gb_s[oi, hh * GRID_W:(hh + 1) * GRID_W, :])
            back = pltpu.roll(g, KWIN - (GRID_W - WIN_W), 1, stride=1, stride_axis=0)
            out_ref[0, oi, hh:hh + 1, :] = jnp.sum(back, axis=0, keepdims=True)


def _rpb_fold(row_grads):
    g = row_grads.transpose(1, 0, 2, 3).reshape(WIN_H, N_HEADS, WIN_H, GRID_W)
    g = g.transpose(0, 2, 1, 3)

    def body(g_ref, o_ref):
        for dr in range(N_RPB_R):
            terms = [g_ref[oi, i] for oi in range(WIN_H) for i in range(WIN_H) if i - oi + WIN_H - 1 == dr]
            acc = terms[0]
            for term in terms[1:]:
                acc = acc + term
            o_ref[dr] = acc

    out = pl.pallas_call(
        body, name="rpb_fold",
        out_shape=jax.ShapeDtypeStruct((N_RPB_R, N_HEADS, GRID_W), F32),
    )(g)
    return out.transpose(1, 0, 2)[:, :, :N_RPB_C]


ATT_GROUPS = N_HEADS // HG
ATT_UNROLL = 2


def _stacked(rows64, same_head):
    return jnp.where(same_head, jnp.concatenate([rows64] * HG, axis=0), jnp.zeros((), BF16))


def _own_heads(stacked):
    head = lax.broadcasted_iota(jnp.int32, (GRID_W, HC), 1) // DH
    out = stacked[:GRID_W]
    for h in range(1, HG):
        out = jnp.where(head == h, stacked[h * GRID_W:(h + 1) * GRID_W], out)
    return out


def _att_scores(q_ref, k_ref, bias_ref, same_head, r):
    rs = jnp.clip(r - WIN_H // 2, 0, ROWS - WIN_H)
    oi = r - rs
    q0 = pl.multiple_of(r * GRID_W, GRID_W)
    k0 = pl.multiple_of(rs * GRID_W, GRID_W)
    q2 = _stacked(q_ref[pl.ds(q0, GRID_W), :] * (DH ** -0.5), same_head)
    kw = k_ref[pl.ds(k0, KWIN), :]
    s = _dot_nt(q2, kw) + bias_ref[oi]
    e = jnp.exp(s - jnp.max(s, axis=-1, keepdims=True))
    return e, 1.0 / jnp.sum(e, axis=-1, keepdims=True), q2, kw, q0, k0, oi


def _att_specs():
    col = lambda off: pl.BlockSpec((T, HC), lambda g: (0, g + off * ATT_GROUPS))
    tables = [pl.BlockSpec((WIN_H, 1, HG, KWIN), lambda g: (0, g, 0, 0)),
              pl.BlockSpec((GRID_W, KWIN), lambda g: (0, 0)),
              pl.BlockSpec((HQ, HC), lambda g: (0, 0))]
    return col, tables, pltpu.VMEM((WIN_H, HQ, KWIN), F32)


def _att_fwd(qkv, bias_rows):
    valid_np, same_head_np = _att_tables()

    def body(q_ref, k_ref, v_ref, rows_ref, valid_ref, head_ref, o_ref, bias_s):
        same_head = head_ref[...] > 0.5
        _bias_tiles(rows_ref, valid_ref[...] > 0.5, bias_s)

        def row(r, carry):
            e, rl, _, _, q0, k0, _ = _att_scores(q_ref, k_ref, bias_s, same_head, r)
            o2 = _dot((e * rl).astype(BF16), v_ref[pl.ds(k0, KWIN), :])
            o_ref[pl.ds(q0, GRID_W), :] = _own_heads(o2).astype(BF16)
            return carry

        lax.fori_loop(0, ROWS, row, 0, unroll=ATT_UNROLL)

    col, tables, tiles = _att_specs()
    return pl.pallas_call(
        body, name="att_fwd",
        out_shape=jax.ShapeDtypeStruct((T, D_ATT), BF16),
        grid=(ATT_GROUPS,),
        in_specs=[col(0), col(1), col(2)] + tables,
        out_specs=col(0),
        scratch_shapes=[tiles],
        compiler_params=_params(dimension_semantics=("parallel",)),
    )(qkv, qkv, qkv, bias_rows, jnp.asarray(valid_np), jnp.asarray(same_head_np))


def _att_bwd(qkv, bias_rows, datt, after):
    valid_np, same_head_np = _att_tables()

    def body(q_ref, k_ref, v_ref, do_ref, rows_ref, valid_ref, head_ref, flip_ref,
             dqkv_ref, grows_ref, dk_acc, dv_acc, bias_s, gb_s):
        same_head = head_ref[...] > 0.5
        dk_acc[...] = jnp.zeros_like(dk_acc)
        dv_acc[...] = jnp.zeros_like(dv_acc)
        gb_s[...] = jnp.zeros_like(gb_s)
        _bias_tiles(rows_ref, valid_ref[...] > 0.5, bias_s)

        def row(r, carry):
            e, rl, q2, kw, q0, k0, oi = _att_scores(q_ref, k_ref, bias_s, same_head, r)
            do2 = _stacked(do_ref[pl.ds(q0, GRID_W), :], same_head)
            vw = v_ref[pl.ds(k0, KWIN), :]
            p = e * rl
            dp = _dot_nt(do2, vw)
            ds = p * (dp - jnp.sum(dp * p, axis=-1, keepdims=True))
            p16 = p.astype(BF16)
            ds16 = ds.astype(BF16)
            dv_acc[pl.ds(k0, KWIN), :] += _dot_tn(p16, do2)
            dk_acc[pl.ds(k0, KWIN), :] += _dot_tn(ds16, q2)
            dq2 = _dot(ds16, kw) * (DH ** -0.5)
            dqkv_ref[0, pl.ds(q0, GRID_W), :] = _own_heads(dq2).astype(BF16)
            gb_s[oi] += ds
            return carry

        lax.fori_loop(0, ROWS, row, 0, unroll=ATT_UNROLL)
        dqkv_ref[1] = dk_acc[...].astype(BF16)
        dqkv_ref[2] = dv_acc[...].astype(BF16)
        _bias_tile_grads(gb_s, flip_ref[...], grows_ref)

    col, tables, tiles = _att_specs()
    return pl.pallas_call(
        body, name="att_bwd",
        out_shape=(jax.ShapeDtypeStruct((3, T, D_ATT), BF16),
                   jax.ShapeDtypeStruct((ATT_GROUPS, WIN_H, HG, KWIN), F32)),
        grid=(ATT_GROUPS,),
        in_specs=[col(0), col(1), col(2), col(0)] + tables + [pl.BlockSpec((GRID_W, GRID_W), lambda g: (0, 0))],
        out_specs=(pl.BlockSpec((3, T, HC), lambda g: (0, 0, g)),
                   pl.BlockSpec((1, WIN_H, HG, KWIN), lambda g: (g, 0, 0, 0))),
        scratch_shapes=[pltpu.VMEM((T, HC), F32), pltpu.VMEM((T, HC), F32), tiles, tiles],
        compiler_params=_params(dimension_semantics=("parallel",)),
    )(qkv, qkv, qkv, datt, bias_rows, jnp.asarray(valid_np) + after, jnp.asarray(same_head_np),
      jnp.asarray(np.eye(GRID_W, dtype=np.float32)[::-1].copy()))


def _conv_taps(up):
    return (_shift_rows(up, 2), _shift_rows(up, 1), up, _shift_rows(up, -1))


def _pair_block_diag(w_pair, dup, same_half):
    return jnp.where(same_half, _dot(w_pair.astype(BF16), dup), 0.0).astype(BF16)


def _gates(u, u16, wa, ba, wi, bi, lam):
    r = _sigmoid(_dot(u16, wa) + ba)
    ig = _sigmoid(_dot(u16, wi) + bi)
    sp = _softplus(-lam)
    log_a = (-LRU_C) * r * sp
    a = jnp.exp(log_a)
    mult2 = jnp.maximum(_one_minus_square(log_a, a), 0.0)
    return r, ig, sp, a, jnp.sqrt(mult2), mult2


SCAN_BLOCKS = 2


def _scans(jobs):
    c = jobs[0][0].shape[1]
    nblk = T // 8
    rows = lax.broadcasted_iota(jnp.int32, (8, c), 0)

    def block(a, b, reverse):
        for s in (1, 2, 4):
            if reverse:
                keep = rows < 8 - s
                a_s = jnp.where(keep, pltpu.roll(a, 8 - s, 0), 1.0)
                b_s = jnp.where(keep, pltpu.roll(b, 8 - s, 0), 0.0)
            else:
                keep = rows >= s
                a_s = jnp.where(keep, pltpu.roll(a, s, 0), 1.0)
                b_s = jnp.where(keep, pltpu.roll(b, s, 0), 0.0)
            b = a * b_s + b
            a = a * a_s
        return a, b

    def step(i, carry):
        out = []
        for (a_ref, b_ref, h_ref, reverse), h_prev in zip(jobs, carry):
            for u in range(SCAN_BLOCKS):
                blk = i * SCAN_BLOCKS + u
                if reverse:
                    blk = nblk - 1 - blk
                t0 = pl.multiple_of(blk * 8, 8)
                a, b = block(a_ref[pl.ds(t0, 8), :], b_ref[pl.ds(t0, 8), :], reverse)
                h = a * h_prev + b
                h_ref[pl.ds(t0, 8), :] = h
                h_prev = jnp.broadcast_to(h[0:1] if reverse else h[7:8], (8, c))
            out.append(h_prev)
        return tuple(out)

    lax.fori_loop(0, nblk // SCAN_BLOCKS, step, tuple(jnp.zeros((8, c), F32) for _ in jobs))


def _rec_specs():
    tok = lambda off: pl.BlockSpec((T, CG), lambda g: (0, g + off))
    per_ch = lambda rows: pl.BlockSpec((rows, CG), lambda g: (0, g))
    wspec = pl.BlockSpec((2, 1, CG, REC_BLOCK), lambda g: (0, g, 0, 0))
    const = lambda shape: pl.BlockSpec(shape, lambda g: (0, 0))
    return tok, per_ch, wspec, const


def _rec_fwd(uy, conv_w, conv_b, w_a, b_a, w_i, b_i, lam):
    tok, per_ch, wspec, const = _rec_specs()

    def body(up_ref, yb_ref, cw_ref, cb_ref, wa_ref, ba_ref, wi_ref, bi_ref, lam_ref, dup_ref, half_ref,
             hf_ref, hb_ref, yrec_ref, am_ref, bx_f, bx_b):
        dup = dup_ref[...]
        same_half = half_ref[...] > 0.5
        taps = _conv_taps(up_ref[...])
        u = cb_ref[...]
        for j in range(4):
            u = u + taps[j] * cw_ref[j:j + 1, :]
        u16 = u.astype(BF16)
        for d, bx_s in enumerate((bx_f, bx_b)):
            wa = _pair_block_diag(wa_ref[d, 0], dup, same_half)
            wi = _pair_block_diag(wi_ref[d, 0], dup, same_half)
            _, ig, _, a, mult, _ = _gates(u, u16, wa, ba_ref[d:d + 1, :], wi, bi_ref[d:d + 1, :],
                                       lam_ref[d:d + 1, :])
            am_ref[2 * d] = a
            am_ref[2 * d + 1] = mult
            bx_s[...] = mult * (ig * u)
        _scans([(am_ref.at[0], bx_f, hf_ref, False), (am_ref.at[2], bx_b, hb_ref, True)])
        gelu, _ = _gelu_and_grad(yb_ref[...])
        yrec_ref[...] = ((hf_ref[...] + hb_ref[...]) * gelu).astype(BF16)

    return pl.pallas_call(
        body, name="rec_fwd",
        out_shape=(jax.ShapeDtypeStruct((T, D_REC), F32), jax.ShapeDtypeStruct((T, D_REC), F32),
                   jax.ShapeDtypeStruct((T, D_REC), BF16), jax.ShapeDtypeStruct((4, T, D_REC), F32)),
        grid=(N_CG,),
        in_specs=[tok(0), tok(N_CG), per_ch(4), per_ch(1), wspec, per_ch(2), wspec, per_ch(2), per_ch(2),
                  const((REC_BLOCK, CG)), const((CG, CG))],
        out_specs=(tok(0), tok(0), tok(0), pl.BlockSpec((4, T, CG), lambda g: (0, 0, g))),
        scratch_shapes=[pltpu.VMEM((T, CG), F32)] * 2,
        compiler_params=_params(dimension_semantics=("parallel",)),
    )(uy, uy, conv_w, conv_b, w_a, b_a, w_i, b_i, lam,
      jnp.asarray(_dup_table(), BF16), jnp.asarray(_pair_mask()))


def _rec_bwd(uy, hf, hb, am, dyrec, conv_w, conv_b, w_a, b_a, w_i, b_i, lam):
    tok, per_ch, wspec, const = _rec_specs()

    def body(up_ref, yb_ref, hf_ref, hb_ref, am_ref, dy_ref, cw_ref, cb_ref, wa_ref, ba_ref, wi_ref, bi_ref,
             lam_ref, dup_ref, dupt_ref, half_ref,
             duy_ref, dcw_ref, dcb_ref, dwa_ref, dba_ref, dwi_ref, dbi_ref, dlam_ref,
             a_s0, a_s1, dh_s, g_s0, g_s1):
        dup = dup_ref[...]
        dup_t = dupt_ref[...]
        same_half = half_ref[...] > 0.5
        taps = _conv_taps(up_ref[...])
        u = cb_ref[...]
        for j in range(4):
            u = u + taps[j] * cw_ref[j:j + 1, :]
        u16 = u.astype(BF16)
        gelu, dgelu = _gelu_and_grad(yb_ref[...])
        dy = dy_ref[...]
        duy_ref[1] = (dy * (hf_ref[...] + hb_ref[...]) * dgelu).astype(BF16)
        dh_s[...] = dy * gelu
        a_s0[...] = _shift_rows(am_ref[0], -1)
        a_s1[...] = _shift_rows(am_ref[2], 1)
        _scans([(a_s0, dh_s, g_s0, True), (a_s1, dh_s, g_s1, False)])
        du = jnp.zeros((T, CG), F32)
        for d, g_s in enumerate((g_s0, g_s1)):
            reverse = d == 1
            wa = _pair_block_diag(wa_ref[d, 0], dup, same_half)
            wi = _pair_block_diag(wi_ref[d, 0], dup, same_half)
            lam_d = lam_ref[d:d + 1, :]
            r = _sigmoid(_dot(u16, wa) + ba_ref[d:d + 1, :])
            ig = _sigmoid(_dot(u16, wi) + bi_ref[d:d + 1, :])
            sp = _softplus(-lam_d)
            a, mult = am_ref[2 * d], am_ref[2 * d + 1]
            mult2 = mult * mult
            g = g_s[...]
            h_prev = _shift_rows(hb_ref[...], -1) if reverse else _shift_rows(hf_ref[...], 1)
            da = g * h_prev
            dmult = g * (ig * u)
            dig = g * mult * u
            du = du + g * mult * ig
            dmult_dlog = jnp.where(mult2 > 0.0, -(a * a) * lax.rsqrt(mult2), 0.0)
            dlog_a = da * a + dmult * dmult_dlog
            dr = dlog_a * ((-LRU_C) * sp)
            dsp = jnp.sum(dlog_a * ((-LRU_C) * r), axis=0, keepdims=True)
            dlam_ref[d:d + 1, :] = dsp * (-_sigmoid(-lam_d))
            dga = dr * r * (1.0 - r)
            dgi = dig * ig * (1.0 - ig)
            dga16 = dga.astype(BF16)
            dgi16 = dgi.astype(BF16)
            du = du + _dot_nt(dga16, wa) + _dot_nt(dgi16, wi)
            dwa_ref[d, 0] = _dot_exact(jnp.where(same_half, _dot_tn(u16, dga16), 0.0), dup_t)
            dwi_ref[d, 0] = _dot_exact(jnp.where(same_half, _dot_tn(u16, dgi16), 0.0), dup_t)
            dba_ref[d:d + 1, :] = jnp.sum(dga, axis=0, keepdims=True)
            dbi_ref[d:d + 1, :] = jnp.sum(dgi, axis=0, keepdims=True)
        dcb_ref[...] = jnp.sum(du, axis=0, keepdims=True)
        for j in range(4):
            dcw_ref[j:j + 1, :] = jnp.sum(du * taps[j], axis=0, keepdims=True)
        dup_in = (_shift_rows(du, -2) * cw_ref[0:1, :] + _shift_rows(du, -1) * cw_ref[1:2, :]
                  + du * cw_ref[2:3, :] + _shift_rows(du, 1) * cw_ref[3:4, :])
        duy_ref[0] = dup_in.astype(BF16)

    wshape = jax.ShapeDtypeStruct((2, N_CG, CG, REC_BLOCK), F32)
    vec = lambda rows: jax.ShapeDtypeStruct((rows, D_REC), F32)
    dup_np = _dup_table()
    return pl.pallas_call(
        body, name="rec_bwd",
        out_shape=(jax.ShapeDtypeStruct((2, T, D_REC), BF16),
                   vec(4), vec(1), wshape, vec(2), wshape, vec(2), vec(2)),
        grid=(N_CG,),
        in_specs=[tok(0), tok(N_CG), tok(0), tok(0), pl.BlockSpec((4, T, CG), lambda g: (0, 0, g)), tok(0),
                  per_ch(4), per_ch(1), wspec, per_ch(2), wspec, per_ch(2), per_ch(2),
                  const((REC_BLOCK, CG)), const((CG, REC_BLOCK)), const((CG, CG))],
        out_specs=(pl.BlockSpec((2, T, CG), lambda g: (0, 0, g)),
                   per_ch(4), per_ch(1), wspec, per_ch(2), wspec, per_ch(2), per_ch(2)),
        scratch_shapes=[pltpu.VMEM((T, CG), F32)] * 5,
        compiler_params=_params(dimension_semantics=("parallel",)),
    )(uy, uy, hf, hb, am, dyrec, conv_w, conv_b, w_a, b_a, w_i, b_i, lam,
      jnp.asarray(dup_np, BF16), jnp.asarray(dup_np.T.copy()), jnp.asarray(_pair_mask()))


TM_MIX = 256


def _mix_specs():
    tok = lambda width, blk=0: pl.BlockSpec((TM_MIX, width), lambda i: (i, blk))
    full = lambda shape: pl.BlockSpec(shape, lambda i: (0, 0))
    return tok, full


def _mix_fwd(x, att, yrec, gg, w_att_o_t, w_rec_o, w_out):
    tok, full = _mix_specs()

    def body(x_ref, att_ref, yr_ref, ga_ref, gr_ref, wao_ref, wro_ref, wo_ref, x1_ref, mixed_ref):
        y_att = _dot_nt(att_ref[...], wao_ref[...])
        y_rec = _dot(yr_ref[...], wro_ref[...])
        mixed = (_sigmoid(ga_ref[...]) * y_att + _sigmoid(gr_ref[...]) * y_rec).astype(BF16)
        mixed_ref[...] = mixed
        x1_ref[...] = x_ref[...] + _dot(mixed, wo_ref[...])

    return pl.pallas_call(
        body, name="mix_fwd",
        out_shape=(jax.ShapeDtypeStruct((T, D), F32), jax.ShapeDtypeStruct((T, D), BF16)),
        grid=(T // TM_MIX,),
        in_specs=[tok(D), tok(D_ATT), tok(D_REC), tok(D, 0), tok(D, 1),
                  full((D, D_ATT)), full((D_REC, D)), full((D, D))],
        out_specs=(tok(D), tok(D)),
        compiler_params=_params(dimension_semantics=("parallel",)),
    )(x, att, yrec, gg, gg, w_att_o_t, w_rec_o, w_out)


def _mix_bwd(dx1, att, yrec, gg, w_att_o_t, w_rec_o, w_out):
    tok, full = _mix_specs()

    def body(dx_ref, att_ref, yr_ref, ga_ref, gr_ref, wao_ref, wro_ref, wo_ref,
             dgg_ref, dya_ref, dyr_ref, datt_ref, dyrp_ref):
        dmixed = _dot_nt(dx_ref[...].astype(BF16), wo_ref[...])
        y_att = _dot_nt(att_ref[...], wao_ref[...])
        y_rec = _dot(yr_ref[...], wro_ref[...])
        sa = _sigmoid(ga_ref[...])
        sr = _sigmoid(gr_ref[...])
        dgg_ref[0] = (dmixed * y_att * sa * (1.0 - sa)).astype(BF16)
        dgg_ref[1] = (dmixed * y_rec * sr * (1.0 - sr)).astype(BF16)
        dya = (dmixed * sa).astype(BF16)
        dyr = (dmixed * sr).astype(BF16)
        dya_ref[...] = dya
        dyr_ref[...] = dyr
        datt_ref[...] = _dot(dya, wao_ref[...]).astype(BF16)
        dyrp_ref[...] = _dot_nt(dyr, wro_ref[...])

    return pl.pallas_call(
        body, name="mix_bwd",
        out_shape=(jax.ShapeDtypeStruct((2, T, D), BF16),
                   jax.ShapeDtypeStruct((T, D), BF16), jax.ShapeDtypeStruct((T, D), BF16),
                   jax.ShapeDtypeStruct((T, D_ATT), BF16), jax.ShapeDtypeStruct((T, D_REC), F32)),
        grid=(T // TM_MIX,),
        in_specs=[tok(D), tok(D_ATT), tok(D_REC), tok(D, 0), tok(D, 1),
                  full((D, D_ATT)), full((D_REC, D)), full((D, D))],
        out_specs=(pl.BlockSpec((2, TM_MIX, D), lambda i: (0, i, 0)),
                   tok(D), tok(D), tok(D_ATT), tok(D_REC)),
        compiler_params=_params(dimension_semantics=("parallel",)),
    )(dx1, att, yrec, gg, gg, w_att_o_t, w_rec_o, w_out)


TM_FFN = 256
FF_CHUNK = 1024


def _ffn_loss(x1, target, g2, gf, w_ff1_t, w_ff2):
    n_chunks = D_FF // FF_CHUNK

    def body(x1_ref, tg_ref, g2_ref, gf_ref, w1_hbm, w2_hbm,
             loss_ref, dx1_ref, h2_ref, act_ref, dpre_ref, dx2_ref, dg2_ref, dgf_ref,
             w1, w2, relu_s):
        i = pl.program_id(0)

        @pl.when(i == 0)
        def _():
            pltpu.sync_copy(w1_hbm, w1)
            pltpu.sync_copy(w2_hbm, w2)
            loss_ref[...] = jnp.zeros_like(loss_ref)
            dg2_ref[...] = jnp.zeros_like(dg2_ref)
            dgf_ref[...] = jnp.zeros_like(dgf_ref)

        x1v = x1_ref[...]
        r2 = lax.rsqrt(jnp.mean(x1v * x1v, axis=-1, keepdims=True) + EPS)
        xh2 = x1v * r2
        h2 = (xh2 * g2_ref[...]).astype(BF16)
        h2_ref[...] = h2
        x2 = x1v
        for c in range(n_chunks):
            ff = slice(c * FF_CHUNK, (c + 1) * FF_CHUNK)
            rl = jnp.maximum(_dot_nt(h2, w1[ff, :]), 0.0)
            relu_s[:, ff] = rl
            act = (rl * rl).astype(BF16)
            act_ref[:, ff] = act
            x2 = x2 + _dot(act, w2[ff, :])
        r3 = lax.rsqrt(jnp.mean(x2 * x2, axis=-1, keepdims=True) + EPS)
        xh3 = x2 * r3
        err = xh3 * gf_ref[...] - tg_ref[...]
        loss_ref[...] += 0.5 * jnp.sum(jnp.mean(err * err, axis=-1, keepdims=True))
        dy = err * (1.0 / D)
        dgf_ref[...] += jnp.sum(dy * xh3, axis=0, keepdims=True)
        dx2 = _rms_bwd(dy, xh3, r3, gf_ref[...])
        dx2_16 = dx2.astype(BF16)
        dx2_ref[...] = dx2_16
        dh2 = jnp.zeros((TM_FFN, D), F32)
        for c in range(n_chunks):
            ff = slice(c * FF_CHUNK, (c + 1) * FF_CHUNK)
            dpre = (_dot_nt(dx2_16, w2[ff, :]) * (2.0 * relu_s[:, ff])).astype(BF16)
            dpre_ref[:, ff] = dpre
            dh2 = dh2 + _dot(dpre, w1[ff, :])
        dg2_ref[...] += jnp.sum(dh2 * xh2, axis=0, keepdims=True)
        dx1_ref[...] = dx2 + _rms_bwd(dh2, xh2, r2, g2_ref[...])

    tok = lambda width: pl.BlockSpec((TM_FFN, width), lambda i: (i, 0))
    vec = pl.BlockSpec((1, D), lambda i: (0, 0))
    hbm = pl.BlockSpec(memory_space=pl.ANY)
    return pl.pallas_call(
        body, name="ffn_loss",
        out_shape=(jax.ShapeDtypeStruct((8, 128), F32), jax.ShapeDtypeStruct((T, D), F32),
                   jax.ShapeDtypeStruct((T, D), BF16), jax.ShapeDtypeStruct((T, D_FF), BF16),
                   jax.ShapeDtypeStruct((T, D_FF), BF16), jax.ShapeDtypeStruct((T, D), BF16),
                   jax.ShapeDtypeStruct((1, D), F32), jax.ShapeDtypeStruct((1, D), F32)),
        grid=(T // TM_FFN,),
        in_specs=[tok(D), tok(D), vec, vec, hbm, hbm],
        out_specs=(pl.BlockSpec((8, 128), lambda i: (0, 0)), tok(D), tok(D), tok(D_FF), tok(D_FF), tok(D),
                   vec, vec),
        scratch_shapes=[pltpu.VMEM((D_FF, D), BF16), pltpu.VMEM((D_FF, D), BF16),
                        pltpu.VMEM((TM_FFN, D_FF), F32)],
        compiler_params=_params(dimension_semantics=("arbitrary",)),
    )(x1, target, g2, gf, w_ff1_t, w_ff2)


def _local_step(x, target, p, late_weights, reduce_early):
    bias = _rpb_rows(p["rpb"])
    pairs = lambda w: w.reshape(2, N_CG, CG, REC_BLOCK)
    w_a, w_i = pairs(p["w_rg_a"]), pairs(p["w_rg_i"])
    rec_params = (p["conv_w"], p["conv_b"], w_a, p["b_rg_a"], w_i, p["b_rg_i"], p["lru_lambda"])

    qkv, uy, gg, h = _in_proj(x, p["ln1_g"], p["w_in_t"], p["b_in"])
    att = _att_fwd(qkv, bias)
    hf, hb, yrec, am = _rec_fwd(uy, *rec_params)
    p = {**p, **late_weights(yrec)}
    x1, mixed = _mix_fwd(x, att, yrec, gg, p["w_att_o_t"], p["w_rec_o"], p["w_out"])
    loss8, dx1, h2, act, dpre, dx2, g_ln2, g_lnf = _ffn_loss(
        x1, target, p["ln2_g"], p["lnf_g"], p["w_ff1_t"], p["w_ff2"])

    dgg, dya, dyr, datt, dyrp = _mix_bwd(dx1, att, yrec, gg, p["w_att_o_t"], p["w_rec_o"], p["w_out"])
    duy, g_cw, g_cb, g_wa, g_ba, g_wi, g_bi, g_lam = _rec_bwd(uy, hf, hb, am, dyrp, *rec_params)
    blocks = lambda g: g.reshape(2, N_REC_BLOCKS, REC_BLOCK, REC_BLOCK)
    grads = {
        "w_att_o_t": _matmul(dya, att, "tn", BF16, "g_w_att_o"),
        "conv_w": g_cw, "conv_b": g_cb, "w_rg_a": blocks(g_wa), "b_rg_a": g_ba,
        "w_rg_i": blocks(g_wi), "b_rg_i": g_bi, "lru_lambda": g_lam,
        "w_rec_o": _matmul(yrec, dyr, "tn", BF16, "g_w_rec_o"),
        "w_out": _matmul(mixed, dx1, "tn", BF16, "g_w_out"),
        "ln2_g": g_ln2,
        "w_ff1_t": _matmul(dpre, h2, "tn", BF16, "g_w_ff1"),
        "w_ff2": _matmul(act, dx2, "tn", BF16, "g_w_ff2"),
        "lnf_g": g_lnf,
    }
    dqkv, gbias = _att_bwd(qkv, bias, datt, reduce_early(grads))
    dz = (dqkv, duy, dgg)
    grad_x, g_ln1 = _dh_norm1_bwd(dz, p["w_in_t"], x, p["ln1_g"], dx1)
    g_w_in_t, g_b_in = _grad_w_in(dz, h)
    grads.update(ln1_g=g_ln1, w_in_t=g_w_in_t, b_in=g_b_in, rpb=_rpb_fold(gbias))
    return loss8[0:1, 0:1], grad_x, grads


MESH_ID = pl.DeviceIdType.MESH
ANY = pl.BlockSpec(memory_space=pl.ANY)

CHAN_BLOCK_ROWS = 32
GATE_ROWS = 2 * 2 * N_REC_BLOCKS * REC_BLOCK * REC_BLOCK // (N_DEV * D)
SECTIONS = (("w_in_t", 704, D), ("w_rec_o", 128, D), ("w_out", 128, D), ("w_ff1_t", 512, D),
            ("w_ff2", 512, D), ("chan", CHAN_BLOCK_ROWS, D), ("w_att_o_t", 128, D_ATT),
            ("gates", GATE_ROWS, D))
N_SEC = len(SECTIONS)
N_CHAN_ROWS = 10
CHAN = (("conv_w", 4), ("b_rg_a", 2), ("b_rg_i", 2), ("lru_lambda", 2))


def _position():
    return lax.axis_index("x"), lax.axis_index("y"), lax.axis_index("c")


def _other_chips(x, y):
    return [(1 - x, y), (x, 1 - y), (1 - x, 1 - y)]


PASS_ON_ID, PAIR_EARLY_ID, PAIR_LATE_ID = 1, 2, 3


def _pair_handshake(x, y, c):
    barrier = pltpu.get_barrier_semaphore()
    pl.semaphore_signal(barrier, inc=1, device_id=(x, y, 1 - c), device_id_type=MESH_ID)
    pl.semaphore_wait(barrier, 1)


def _block_of(ref, dev, rows):
    return ref.at[pl.ds(pl.multiple_of(dev * rows, 16), rows)]


def _all_gather(shards, name):
    ns = len(shards)

    def body(*refs):
        x_refs, out_refs, done_ref = refs[:ns], refs[ns:2 * ns], refs[2 * ns]
        send_sems, recv_sems, local_sems = refs[2 * ns + 1:]
        done_ref[0, 0] = 0.0
        x, y, c = _position()
        me, sibling = (x, y, c), (x, y, 1 - c)
        x_nbr, y_nbr, diagonal = _other_chips(x, y)
        north = c == 1
        relay_from = (jnp.where(north, x_nbr[0], y_nbr[0]), jnp.where(north, x_nbr[1], y_nbr[1]))
        relay_to = (jnp.where(north, y_nbr[0], x_nbr[0]), jnp.where(north, y_nbr[1], x_nbr[1]))

        def rows(s, px, py, pc):
            return _block_of(out_refs[s], 4 * px + 2 * py + pc, shards[s].shape[0])

        def copy(k, s, block, to, from_shard=False):
            return pltpu.make_async_remote_copy(
                src_ref=x_refs[s] if from_shard else rows(s, *block), dst_ref=rows(s, *block),
                send_sem=send_sems.at[k * ns + s], recv_sem=recv_sems.at[k * ns + s],
                device_id=to, device_id_type=MESH_ID)

        sections = range(ns)
        mine = [pltpu.make_async_copy(x_refs[s], rows(s, *me), local_sems.at[s]) for s in sections]
        sent = [copy(k, s, me, to, True) for k, to in enumerate((sibling, (*x_nbr, c), (*y_nbr, c)))
                for s in sections]
        for cp in mine + sent:
            cp.start()
        for s in sections:
            copy(1, s, (*x_nbr, c), me).wait_recv()
            copy(2, s, (*y_nbr, c), me).wait_recv()
            sent += [copy(3, s, (*relay_from, c), (*relay_to, c)),
                     copy(4, s, (*x_nbr, c), sibling), copy(5, s, (*y_nbr, c), sibling)]
            for cp in sent[-3:]:
                cp.start()
        for s in sections:
            copy(3, s, (*diagonal, c), me).wait_recv()
            sent.append(copy(6, s, (*diagonal, c), sibling))
            sent[-1].start()
        for s in sections:
            copy(0, s, sibling, me).wait_recv()
            for k, chip in ((4, x_nbr), (5, y_nbr), (6, diagonal)):
                copy(k, s, (*chip, 1 - c), me).wait_recv()
        for cp in sent:
            cp.wait_send()
        for cp in mine:
            cp.wait()

    return pl.pallas_call(
        body, name=name,
        out_shape=tuple(jax.ShapeDtypeStruct((N_DEV * s.shape[0], s.shape[1]), s.dtype) for s in shards)
        + (jax.ShapeDtypeStruct((1, 1), F32),),
        in_specs=[ANY] * ns,
        out_specs=(ANY,) * ns + (pl.BlockSpec(memory_space=pltpu.SMEM),),
        scratch_shapes=[pltpu.SemaphoreType.DMA((7 * ns,)), pltpu.SemaphoreType.DMA((7 * ns,)),
                        pltpu.SemaphoreType.DMA((ns,))],
    )(*shards)


HBM = pl.BlockSpec(memory_space=pltpu.HBM)
SEM = pl.BlockSpec(memory_space=pltpu.SEMAPHORE)
EFFECT = pltpu.SideEffectType.DATAFLOW_SIDE_EFFECTING


def _in_hbm(a):
    return pltpu.with_memory_space_constraint(a, pltpu.HBM)


def _first_hop_copies(shards, x_refs, zones, send_sems, recv_sems):
    ns = len(shards)
    x, y, c = _position()
    targets = [(x, y, 1 - c)] + [(cx, cy, c) for cx, cy in _other_chips(x, y)]
    return [pltpu.make_async_remote_copy(
        src_ref=x_refs[s], dst_ref=_block_of(zones[s], 4 * x + 2 * y + c, shards[s].shape[0]),
        send_sem=send_sems.at[k * ns + s], recv_sem=recv_sems.at[k * ns + s],
        device_id=to, device_id_type=MESH_ID)
        for k, to in enumerate(targets) for s in range(ns)]


def _after_all(arrays, name):
    def body(*refs):
        refs[-1][...] = jnp.zeros_like(refs[-1])

    return pl.pallas_call(
        body, name=name,
        out_shape=jax.ShapeDtypeStruct((8, LANES), F32),
        in_specs=[pl.BlockSpec(memory_space=pl.ANY)] * len(arrays),
        out_specs=pl.BlockSpec(memory_space=pltpu.VMEM),
    )(*arrays)


def _own_blocks_placed(shards, after):
    ns = len(shards)
    x, y, c = _position()
    me = jnp.reshape(4 * x + 2 * y + c, (1,)).astype(jnp.int32)
    shards = [*shards[:-1], shards[-1] + after.astype(shards[-1].dtype)]

    def body(me_ref, *refs):
        for s in range(ns):
            refs[ns + s][...] = refs[s][...]

    return pl.pallas_call(
        body, name="own_blocks_placed",
        out_shape=tuple(jax.ShapeDtypeStruct((N_DEV * s.shape[0], s.shape[1]), s.dtype) for s in shards),
        grid_spec=pltpu.PrefetchScalarGridSpec(
            num_scalar_prefetch=1, grid=(1,),
            in_specs=[pl.BlockSpec(s.shape, lambda i, me: (0, 0)) for s in shards],
            out_specs=tuple(pl.BlockSpec(s.shape, lambda i, me: (me[0], 0)) for s in shards)),
        compiler_params=_params(dimension_semantics=("arbitrary",)),
    )(me, *shards)


def _gather_start(shards, after, name):
    ns = len(shards)
    zones = _own_blocks_placed(shards, after)

    def body(*refs):
        for cp in _first_hop_copies(shards, refs[:ns], refs[ns:2 * ns], refs[2 * ns], refs[2 * ns + 1]):
            cp.start()
        refs[-1][...] = jnp.zeros_like(refs[-1])

    out = pl.pallas_call(
        body, name=name,
        out_shape=(pltpu.SemaphoreType.DMA((4 * ns,)), pltpu.SemaphoreType.DMA((4 * ns,)),
                   *[pltpu.HBM(a.shape, a.dtype) for a in (*shards, *zones)],
                   jax.ShapeDtypeStruct((8, LANES), F32)),
        in_specs=[HBM] * (2 * ns),
        out_specs=(SEM, SEM, *[HBM] * (2 * ns), pl.BlockSpec(memory_space=pltpu.VMEM)),
        input_output_aliases={i: 2 + i for i in range(2 * ns)},
        compiler_params=pltpu.CompilerParams(has_side_effects=EFFECT),
    )(*[_in_hbm(a) for a in shards], *[_in_hbm(a) for a in zones])
    return out[0], out[1], out[2:2 + ns], out[2 + ns:2 + 2 * ns], out[-1]


def _gather_wait(send_sems, recv_sems, shards, zones, after, name):
    ns = len(shards)

    def body(*refs):
        for cp in _first_hop_copies(shards, refs[:ns], refs[ns:2 * ns], refs[2 * ns], refs[2 * ns + 1]):
            cp.wait_send()
            cp.wait_recv()

    out = pl.pallas_call(
        body, name=name,
        out_shape=tuple(pltpu.HBM(a.shape, a.dtype) for a in (*shards, *zones)),
        in_specs=[HBM] * (2 * ns) + [SEM, SEM, ANY],
        out_specs=(HBM,) * (2 * ns),
        input_output_aliases={i: i for i in range(2 * ns)},
        compiler_params=pltpu.CompilerParams(has_side_effects=EFFECT),
    )(*shards, *zones, send_sems, recv_sems, after)
    return out[ns:]


def _gather_pass_on(rows, zones, name):
    ns = len(zones)

    def body(*refs):
        in_refs, out_refs = refs[:ns], refs[ns:2 * ns]
        send_sems, recv_sems = refs[2 * ns:]
        x, y, c = _position()
        _pair_handshake(x, y, c)
        copies = [pltpu.make_async_remote_copy(
            src_ref=_block_of(in_refs[s], 4 * cx + 2 * cy + c, rows[s]),
            dst_ref=_block_of(out_refs[s], 4 * cx + 2 * cy + c, rows[s]),
            send_sem=send_sems.at[j * ns + s], recv_sem=recv_sems.at[j * ns + s],
            device_id=(x, y, 1 - c), device_id_type=MESH_ID)
            for j, (cx, cy) in enumerate(_other_chips(x, y)) for s in range(ns)]
        for cp in copies:
            cp.start()
        for cp in copies:
            cp.wait_recv()
        for cp in copies:
            cp.wait_send()

    return pl.pallas_call(
        body, name=name,
        out_shape=tuple(jax.ShapeDtypeStruct(z.shape, z.dtype) for z in zones),
        in_specs=[ANY] * ns, out_specs=(ANY,) * ns,
        input_output_aliases={i: i for i in range(ns)},
        scratch_shapes=[pltpu.SemaphoreType.DMA((3 * ns,)), pltpu.SemaphoreType.DMA((3 * ns,))],
        compiler_params=pltpu.CompilerParams(collective_id=PASS_ON_ID),
    )(*zones)


def _pair_copies(sections, g_refs, land, send_sems, recv_sems):
    ns = len(sections)
    x, y, c = _position()
    return [pltpu.make_async_remote_copy(
        src_ref=_block_of(g_refs[s], 2 * k + 1 - c, rows), dst_ref=land[s].at[k],
        send_sem=send_sems.at[k * ns + s], recv_sem=recv_sems.at[k * ns + s],
        device_id=(x, y, 1 - c), device_id_type=MESH_ID)
        for k in range(N_CHIPS) for s, (_, rows, _) in enumerate(sections)]


def _pair_exchange_start(sections, grads, barrier_id, name):
    ns = len(sections)

    def body(*refs):
        _pair_handshake(*_position())
        for cp in _pair_copies(sections, refs[:ns], refs[ns:2 * ns], refs[2 * ns], refs[2 * ns + 1]):
            cp.start()
        refs[-1][...] = jnp.zeros_like(refs[-1])

    zones = [lax.empty((N_CHIPS, rows, cols), BF16) for _, rows, cols in sections]
    n = N_CHIPS * ns
    out = pl.pallas_call(
        body, name=name,
        out_shape=(pltpu.SemaphoreType.DMA((n,)), pltpu.SemaphoreType.DMA((n,)),
                   *[pltpu.HBM(a.shape, a.dtype) for a in (*grads, *zones)],
                   jax.ShapeDtypeStruct((8, LANES), F32)),
        in_specs=[HBM] * (2 * ns),
        out_specs=(SEM, SEM, *[HBM] * (2 * ns), pl.BlockSpec(memory_space=pltpu.VMEM)),
        input_output_aliases={i: 2 + i for i in range(2 * ns)},
        compiler_params=pltpu.CompilerParams(has_side_effects=EFFECT, collective_id=barrier_id),
    )(*[_in_hbm(a) for a in grads], *[_in_hbm(a) for a in zones])
    return out[0], out[1], out[2:2 + ns], out[2 + ns:2 + 2 * ns], out[-1]


def _pair_exchange_wait(sections, send_sems, recv_sems, grads, zones, after, name):
    ns = len(sections)

    def body(*refs):
        for cp in _pair_copies(sections, refs[:ns], refs[ns:2 * ns], refs[2 * ns], refs[2 * ns + 1]):
            cp.wait_send()
            cp.wait_recv()

    out = pl.pallas_call(
        body, name=name,
        out_shape=tuple(pltpu.HBM(a.shape, a.dtype) for a in (*grads, *zones)),
        in_specs=[HBM] * (2 * ns) + [SEM, SEM, ANY],
        out_specs=(HBM,) * (2 * ns),
        input_output_aliases={i: i for i in range(2 * ns)},
        compiler_params=pltpu.CompilerParams(has_side_effects=EFFECT),
    )(*grads, *zones, send_sems, recv_sems, after)
    return out[:ns], out[ns:]


def _pair_add(sections, grads, got, core, name):
    ns = len(sections)

    def body(core_ref, *refs):
        g_refs, got_refs, p_refs = refs[:ns], refs[ns:2 * ns], refs[2 * ns:]
        for s in range(ns):
            p_refs[s][0] = (g_refs[s][...].astype(F32) + got_refs[s][0].astype(F32)).astype(BF16)

    slot = [pl.BlockSpec((1, rows, cols), lambda k, c: (k, 0, 0)) for _, rows, cols in sections]
    return pl.pallas_call(
        body, name=name,
        out_shape=tuple(jax.ShapeDtypeStruct((N_CHIPS, rows, cols), BF16) for _, rows, cols in sections),
        grid_spec=pltpu.PrefetchScalarGridSpec(
            num_scalar_prefetch=1, grid=(N_CHIPS,),
            in_specs=[pl.BlockSpec((rows, cols), lambda k, c: (2 * k + c[0], 0)) for _, rows, cols in sections]
            + slot,
            out_specs=tuple(slot)),
        compiler_params=_params(dimension_semantics=("parallel",)),
    )(core, *grads, *got)


def _chip_copies(sections, p_refs, land, send_sems, recv_sems):
    ns = len(sections)
    x, y, c = _position()
    return [pltpu.make_async_remote_copy(
        src_ref=p_refs[s].at[2 * cx + cy], dst_ref=land[s].at[j],
        send_sem=send_sems.at[j * ns + s], recv_sem=recv_sems.at[j * ns + s],
        device_id=(cx, cy, c), device_id_type=MESH_ID)
        for j, (cx, cy) in enumerate(_other_chips(x, y)) for s in range(ns)]


def _chip_exchange(sections, parts, name):
    ns = len(sections)

    def body(*refs):
        copies = _chip_copies(sections, refs[:ns], refs[ns:2 * ns], *refs[2 * ns:])
        for cp in copies:
            cp.start()
        for cp in copies:
            cp.wait_recv()
        for cp in copies:
            cp.wait_send()

    n = 3 * ns
    return pl.pallas_call(
        body, name=name,
        out_shape=tuple(jax.ShapeDtypeStruct((3, rows, cols), BF16) for _, rows, cols in sections),
        in_specs=[ANY] * ns, out_specs=(ANY,) * ns,
        scratch_shapes=[pltpu.SemaphoreType.DMA((n,)), pltpu.SemaphoreType.DMA((n,))],
    )(*parts)


def _chip_exchange_start(sections, parts, name):
    ns = len(sections)

    def body(*refs):
        p_refs, land = refs[:ns], refs[ns:2 * ns]
        send_sems, recv_sems = refs[2 * ns], refs[2 * ns + 1]
        token = refs[-1]
        for cp in _chip_copies(sections, p_refs, land, send_sems, recv_sems):
            cp.start()
        token[...] = jnp.zeros_like(token)

    zones = [lax.empty((3, rows, cols), BF16) for _, rows, cols in sections]
    out = pl.pallas_call(
        body, name=name,
        out_shape=(pltpu.SemaphoreType.DMA((3 * ns,)), pltpu.SemaphoreType.DMA((3 * ns,)),
                   *[pltpu.HBM(a.shape, a.dtype) for a in parts], *[pltpu.HBM(a.shape, a.dtype) for a in zones],
                   jax.ShapeDtypeStruct((8, LANES), F32)),
        in_specs=[HBM] * (2 * ns),
        out_specs=(SEM, SEM, *[HBM] * (2 * ns), pl.BlockSpec(memory_space=pltpu.VMEM)),
        input_output_aliases={i: 2 + i for i in range(2 * ns)},
        compiler_params=pltpu.CompilerParams(has_side_effects=EFFECT),
    )(*[_in_hbm(a) for a in parts], *[_in_hbm(a) for a in zones])
    return out[0], out[1], out[2:2 + ns], out[2 + ns:2 + 2 * ns], out[-1]


def _chip_exchange_wait(sections, send_sems, recv_sems, parts, zones, after, name):
    ns = len(sections)

    def body(*refs):
        p_refs, land = refs[:ns], refs[ns:2 * ns]
        for cp in _chip_copies(sections, p_refs, land, refs[2 * ns], refs[2 * ns + 1]):
            cp.wait_send()
            cp.wait_recv()

    out = pl.pallas_call(
        body, name=name,
        out_shape=tuple(pltpu.HBM(a.shape, a.dtype) for a in (*parts, *zones)),
        in_specs=[HBM] * (2 * ns) + [SEM, SEM, ANY],
        out_specs=(HBM,) * (2 * ns),
        input_output_aliases={i: i for i in range(2 * ns)},
        compiler_params=pltpu.CompilerParams(has_side_effects=EFFECT),
    )(*parts, *zones, send_sems, recv_sems, after)
    return out[:ns], out[ns:]


def _grad_finish(sections, parts, far, chip, name):
    ns = len(sections)

    def body(chip_ref, *refs):
        p_refs, b_refs, g_refs = refs[:ns], refs[ns:2 * ns], refs[2 * ns:]
        for s in range(ns):
            g = p_refs[s][0].astype(F32)
            for j in range(3):
                g = g + b_refs[s][j].astype(F32)
            g_refs[s][...] = g

    half = [(rows // 2, cols) for _, rows, cols in sections]
    return pl.pallas_call(
        body, name=name,
        out_shape=tuple(jax.ShapeDtypeStruct((rows, cols), F32) for _, rows, cols in sections),
        grid_spec=pltpu.PrefetchScalarGridSpec(
            num_scalar_prefetch=1, grid=(2,),
            in_specs=[pl.BlockSpec((1, r, c), lambda i, chip: (chip[0], i, 0)) for r, c in half]
            + [pl.BlockSpec((3, r, c), lambda i, chip: (0, i, 0)) for r, c in half],
            out_specs=tuple(pl.BlockSpec((r, c), lambda i, chip: (i, 0)) for r, c in half)),
        compiler_params=_params(dimension_semantics=("parallel",)),
    )(chip, *parts, *far)


def _sum_devices(parts, rows, name):
    cols = parts.shape[1]
    tr = rows // 2

    def body(*refs):
        s = refs[0][...].astype(F32)
        for d in range(1, N_DEV):
            s = s + refs[d][...].astype(F32)
        refs[N_DEV][...] = s

    return pl.pallas_call(
        body, name=name,
        out_shape=jax.ShapeDtypeStruct((rows, cols), F32),
        grid=(2,),
        in_specs=[pl.BlockSpec((tr, cols), lambda i, d=d: (2 * d + i, 0)) for d in range(N_DEV)],
        out_specs=pl.BlockSpec((tr, cols), lambda i: (i, 0)),
        compiler_params=_params(dimension_semantics=("parallel",)),
    )(*([parts] * N_DEV))


def _adamw_step(w_ref, g_ref, m_ref, v_ref, d_ref, nm_ref, nv_ref):
    c1 = 1.0 / (1.0 - ADAM_B1 ** ADAM_STEP)
    c2 = 1.0 / (1.0 - ADAM_B2 ** ADAM_STEP)
    gv = g_ref[...]
    nm = ADAM_B1 * m_ref[...] + (1.0 - ADAM_B1) * gv
    nv = ADAM_B2 * v_ref[...] + (1.0 - ADAM_B2) * (gv * gv)
    nm_ref[...] = nm
    nv_ref[...] = nv
    d_ref[...] = (-ADAM_LR) * ((nm * c1) / (jnp.sqrt(nv * c2) + ADAM_EPS) + ADAM_WD * w_ref[...])


def _adamw_small(params, name):
    n = len(params)

    def body(*refs):
        for k in range(n):
            _adamw_step(*refs[4 * k:4 * k + 4], *refs[4 * n + 3 * k:4 * n + 3 * k + 3])

    out = pl.pallas_call(
        body, name=name,
        out_shape=tuple(jax.ShapeDtypeStruct(p[0].shape, F32) for p in params for _ in range(3)),
    )(*[a for p in params for a in p])
    return [out[3 * k:3 * k + 3] for k in range(n)]


def _adamw(w, g, m, v, name):
    rows, cols = w.shape
    tr = rows
    while tr * cols * 4 > (1 << 20) and tr % 16 == 0:
        tr //= 2

    def body(*refs):
        _adamw_step(*refs)

    spec = pl.BlockSpec((tr, cols), lambda i: (i, 0))
    shape = jax.ShapeDtypeStruct((rows, cols), F32)
    return pl.pallas_call(
        body, name=name,
        out_shape=(shape, shape, shape),
        grid=(rows // tr,),
        in_specs=[spec] * 4, out_specs=(spec,) * 3,
        compiler_params=_params(dimension_semantics=("parallel",)),
    )(w, g, m, v)


NAMES = ("ln1_g", "w_in", "b_in", "rpb", "w_att_o", "conv_w", "conv_b", "w_rg_a", "b_rg_a", "w_rg_i",
         "b_rg_i", "lru_lambda", "w_rec_o", "w_out", "ln2_g", "w_ff1", "w_ff2", "lnf_g")
TRANSPOSED = {"w_in": "w_in_t", "w_att_o": "w_att_o_t", "w_ff1": "w_ff1_t"}
ROW_SHARDED = ("w_rec_o", "w_out", "w_ff2")
REPLICATED = (("ln1_g", (1, D)), ("b_in", (1, D_IN)), ("rpb", (N_HEADS * N_RPB_R, N_RPB_C)),
              ("conv_b", (1, D_REC)), ("w_rg_a", (2 * N_REC_BLOCKS * REC_BLOCK, REC_BLOCK)),
              ("w_rg_i", (2 * N_REC_BLOCKS * REC_BLOCK, REC_BLOCK)), ("ln2_g", (1, D)), ("lnf_g", (1, D)))
GATE_BLOCKS = ("w_rg_a", "w_rg_i")
SMALL_ROWS = 112


def _chan_bits(vectors):
    chan = jnp.concatenate(vectors, axis=0)
    bits = lax.bitcast_convert_type(chan, BF16).reshape(-1)
    return jnp.pad(bits, (0, CHAN_BLOCK_ROWS * D - bits.shape[0])).reshape(CHAN_BLOCK_ROWS, D)


def _chan_from_bits(gathered):
    bits = gathered.reshape(N_DEV, CHAN_BLOCK_ROWS * D)[:, :2 * N_CHAN_ROWS * LANES]
    chan = lax.bitcast_convert_type(bits.reshape(N_DEV, N_CHAN_ROWS, LANES, 2), F32)
    return chan.transpose(1, 0, 2).reshape(N_CHAN_ROWS, D)


def kernel(x, ln1_g, w_in, b_in, rpb, w_att_o, conv_w, conv_b, w_rg_a, b_rg_a, w_rg_i, b_rg_i, lru_lambda, w_rec_o, w_out, ln2_g, w_ff1, w_ff2, lnf_g, loss_target, m_ln1_g, m_w_in, m_b_in, m_rpb, m_w_att_o, m_conv_w, m_conv_b, m_w_rg_a, m_b_rg_a, m_w_rg_i, m_b_rg_i, m_lru_lambda, m_w_rec_o, m_w_out, m_ln2_g, m_w_ff1, m_w_ff2, m_lnf_g, v_ln1_g, v_w_in, v_b_in, v_rpb, v_w_att_o, v_conv_w, v_conv_b, v_w_rg_a, v_b_rg_a, v_w_rg_i, v_b_rg_i, v_lru_lambda, v_w_rec_o, v_w_out, v_ln2_g, v_w_ff1, v_w_ff2, v_lnf_g):
    w = dict(zip(NAMES, (ln1_g, w_in, b_in, rpb, w_att_o, conv_w, conv_b, w_rg_a, b_rg_a, w_rg_i,
                         b_rg_i, lru_lambda, w_rec_o, w_out, ln2_g, w_ff1, w_ff2, lnf_g)))
    m = dict(zip(NAMES, (m_ln1_g, m_w_in, m_b_in, m_rpb, m_w_att_o, m_conv_w, m_conv_b, m_w_rg_a,
                         m_b_rg_a, m_w_rg_i, m_b_rg_i, m_lru_lambda, m_w_rec_o, m_w_out, m_ln2_g,
                         m_w_ff1, m_w_ff2, m_lnf_g)))
    v = dict(zip(NAMES, (v_ln1_g, v_w_in, v_b_in, v_rpb, v_w_att_o, v_conv_w, v_conv_b, v_w_rg_a,
                         v_b_rg_a, v_w_rg_i, v_b_rg_i, v_lru_lambda, v_w_rec_o, v_w_out, v_ln2_g,
                         v_w_ff1, v_w_ff2, v_lnf_g)))
    xi, yi, ci = _position()

    shard = {t: w[n][0].T.astype(BF16) for n, t in TRANSPOSED.items()}
    shard.update({n: w[n][0].astype(BF16) for n in ROW_SHARDED})
    shard["chan"] = _chan_bits([w[n][0] for n, _ in CHAN])
    first, later = ("w_in_t", "chan"), ("w_rec_o", "w_out", "w_att_o_t", "w_ff1_t", "w_ff2")
    *gathered, done = _all_gather([shard[n] for n in first], "weight_all_gather")
    p = dict(zip(first, gathered))
    send_sems, recv_sems, sent, zones, token = _gather_start([shard[n] for n in later], done,
                                                             "weight_gather_start")

    def late_weights(after):
        landed = _gather_wait(send_sems, recv_sems, sent, zones, after, "weight_gather_wait")
        return dict(zip(later, _gather_pass_on([shard[n].shape[0] for n in later], landed,
                                               "weight_gather_pass_on")))

    chan = _chan_from_bits(p.pop("chan"))
    r0 = 0
    for n, rows in CHAN:
        p[n] = chan[r0:r0 + rows]
        r0 += rows
    p.update(ln1_g=w["ln1_g"], b_in=w["b_in"] + token[0, 0], rpb=w["rpb"][0], conv_b=w["conv_b"],
             w_rg_a=w["w_rg_a"][0], w_rg_i=w["w_rg_i"][0], ln2_g=w["ln2_g"],
             lnf_g=w["lnf_g"].reshape(1, D))

    core = jnp.reshape(ci, (1,)).astype(jnp.int32)
    chip = jnp.reshape(2 * xi + yi, (1,)).astype(jnp.int32)
    early_sections, late_sections = SECTIONS[1:], SECTIONS[:1]
    in_flight = {}

    def pair_sum_and_send(group, sections, after):
        send_sems, recv_sems, sect, zones, _ = in_flight["pair_" + group]
        sect, got = _pair_exchange_wait(sections, send_sems, recv_sems, sect, zones, after,
                                        "grad_pair_exchange_wait_" + group)
        parts = _pair_add(sections, sect, got, core, "grad_pair_add_" + group)
        in_flight[group] = _chip_exchange_start(sections, parts, "grad_chip_exchange_start_" + group)
        return in_flight[group][-1]

    def reduce_early(grads):
        chan_g = jnp.concatenate([grads[n] for n, _ in CHAN], axis=0)
        chan_g = chan_g.reshape(N_CHAN_ROWS, N_DEV, LANES).transpose(1, 0, 2).astype(BF16)
        chan_g = jnp.pad(chan_g.reshape(N_DEV, -1), ((0, 0), (0, CHAN_BLOCK_ROWS * D - N_CHAN_ROWS * LANES)))
        grads["chan"] = chan_g.reshape(N_DEV * CHAN_BLOCK_ROWS, D)
        grads["gates"] = jnp.concatenate([grads[n].reshape(-1, D) for n in GATE_BLOCKS], axis=0).astype(BF16)
        in_flight["pair_early"] = _pair_exchange_start(
            early_sections, [grads[n] for n, _, _ in early_sections], PAIR_EARLY_ID,
            "grad_pair_exchange_start_early")
        return pair_sum_and_send("early", early_sections, in_flight["pair_early"][-1])[0, 0]

    loss_part, grad_x, grads = _local_step(x[0], loss_target[0], p, late_weights, reduce_early)
    in_flight["pair_late"] = _pair_exchange_start(
        late_sections, [grads[n] for n, _, _ in late_sections], PAIR_LATE_ID, "grad_pair_exchange_start_late")

    def finish(group, sections, after, name):
        send_sems, recv_sems, parts, zones, _ = in_flight[group]
        parts, far = _chip_exchange_wait(sections, send_sems, recv_sems, parts, zones, after,
                                         "grad_chip_exchange_wait_" + name)
        return dict(zip((n for n, _, _ in sections),
                        _grad_finish(sections, parts, far, chip, "grad_finish_" + name)))

    summed = finish("early", early_sections, in_flight["pair_late"][-1], "early")
    started_late = pair_sum_and_send("late", late_sections, summed["gates"])

    flat = jnp.concatenate([grads[n].reshape(-1) for n, _ in REPLICATED if n not in GATE_BLOCKS]
                           + [loss_part.reshape(-1) + started_late[0, 0]])
    n_small = flat.shape[0]
    flat = jnp.pad(flat, (0, SMALL_ROWS * LANES - n_small)).reshape(SMALL_ROWS, LANES)
    small_parts, gate_sum, _ = _all_gather([flat, summed["gates"]], "small_grad_all_gather")
    small = _sum_devices(small_parts, SMALL_ROWS, "small_grad_sum").reshape(-1)
    loss = small[n_small - 1]

    g, delta, new_m, new_v = {}, {}, {}, {}

    def update(n, g2, shape2):
        d2, m2, v2 = _adamw(w[n].reshape(shape2), g2, m[n].reshape(shape2), v[n].reshape(shape2),
                            "adamw_" + n)
        g[n], delta[n], new_m[n], new_v[n] = (a.reshape(w[n].shape) for a in (g2, d2, m2, v2))

    small_params = []
    o = 0
    for n, shape2 in REPLICATED:
        if n in GATE_BLOCKS:
            k, rows = GATE_BLOCKS.index(n), gate_sum.shape[0] // len(GATE_BLOCKS)
            update(n, gate_sum[k * rows:(k + 1) * rows].reshape(shape2), shape2)
        else:
            size = shape2[0] * shape2[1]
            small_params.append((n, small[o:o + size].reshape(shape2), shape2))
            o += size
    chan_back = summed["chan"].reshape(-1)[:N_CHAN_ROWS * LANES].reshape(N_CHAN_ROWS, LANES)
    r0 = 0
    for n, rows in CHAN:
        small_params.append((n, chan_back[r0:r0 + rows], (rows, LANES)))
        r0 += rows
    results = _adamw_small([(w[n].reshape(s2), g2, m[n].reshape(s2), v[n].reshape(s2))
                            for n, g2, s2 in small_params], "adamw_vectors")
    for (n, g2, _), (d2, m2, v2) in zip(small_params, results):
        g[n], delta[n], new_m[n], new_v[n] = (a.reshape(w[n].shape) for a in (g2, d2, m2, v2))

    for n in ROW_SHARDED:
        update(n, summed[n], summed[n].shape)
    for n, t in TRANSPOSED.items():
        if t in summed:
            update(n, summed[t].T, summed[t].shape[::-1])
    summed = finish("late", late_sections, _after_all(list(delta.values()), "updates_done"), "late")
    update("w_in", summed["w_in_t"].T, summed["w_in_t"].shape[::-1])

    return (loss, grad_x[None], *[g[n] for n in NAMES], *[delta[n] for n in NAMES],
            *[new_m[n] for n in NAMES], *[new_v[n] for n in NAMES])
```

```python
import math

import numpy as np
import jax
import jax.numpy as jnp
from jax import lax
from jax.experimental import pallas as pl
from jax.experimental.pallas import tpu as pltpu

F32 = jnp.float32
BF16 = jnp.bfloat16

T = 2048
D = 1024
D_ATT = 512
D_REC = 1024
D_FF = 4096
D_IN = 5632
N_HEADS = 8
DH = 64
GRID_W = 64
ROWS = T // GRID_W
WIN_H = 8
WIN_W = 16
KWIN = WIN_H * GRID_W
N_RPB_R = 2 * WIN_H - 1
N_RPB_C = 2 * WIN_W - 1
N_REC_BLOCKS = 16
REC_BLOCK = 64
CG = 128
N_CG = D_REC // CG
LRU_C = 8.0
EPS = 1e-6
N_DEV = 8
N_CHIPS = 4
LANES = 128

ADAM_LR = 0.001
ADAM_B1 = 0.9
ADAM_B2 = 0.999
ADAM_EPS = 1e-08
ADAM_WD = 0.01
ADAM_STEP = 10

MESH_AXES = ("x", "y", "c")
VMEM_LIMIT = 56 * 1024 * 1024

TILE = 512
DZ_ARRAYS = ((0, 3, 1), (3, 4, 2), (7, 4, 2))
N_DZ_TILES = D_IN // TILE


def _params(**kw):
    return pltpu.CompilerParams(vmem_limit_bytes=VMEM_LIMIT, **kw)


HG = 4
HQ = HG * GRID_W
HC = HG * DH


def _att_tables():
    rq = np.arange(GRID_W)
    kc = np.arange(KWIN) % GRID_W
    win_start = np.clip(rq - WIN_W // 2, 0, GRID_W - WIN_W)
    valid = (kc[None, :] >= win_start[:, None]) & (kc[None, :] < win_start[:, None] + WIN_W)
    same_head = (np.arange(HQ)[:, None] // GRID_W) == (np.arange(HC)[None, :] // DH)
    return valid.astype(np.float32), same_head.astype(np.float32)


def _pair_mask():
    half = np.arange(2 * DH) // DH
    return (half[:, None] == half[None, :]).astype(np.float32)


def _dup_table():
    return np.concatenate([np.eye(REC_BLOCK, dtype=np.float32)] * 2, axis=1)


def _sigmoid(x):
    return 0.5 * jnp.tanh(0.5 * x) + 0.5


def _softplus(x):
    return jnp.maximum(x, 0.0) + jnp.log(1.0 + jnp.exp(-jnp.abs(x)))


def _one_minus_square(log_a, a):
    x = 2.0 * log_a
    series = -x * (1.0 + x * (0.5 + x * (1.0 / 6.0)))
    return jnp.where(x > -0.02, series, 1.0 - a * a)


_GELU_C = math.sqrt(2.0 / math.pi)


def _gelu_and_grad(x):
    x2 = x * x
    inner = _GELU_C * (x + 0.044715 * x * x2)
    t = jnp.tanh(inner)
    g = 0.5 * x * (1.0 + t)
    dg = 0.5 * (1.0 + t) + 0.5 * x * (1.0 - t * t) * _GELU_C * (1.0 + 3.0 * 0.044715 * x2)
    return g, dg


def _dot(a, b):
    return jnp.dot(a, b, preferred_element_type=F32)


def _dot_nt(a, b):
    return lax.dot_general(a, b, (((1,), (1,)), ((), ())), preferred_element_type=F32)


def _dot_tn(a, b):
    return lax.dot_general(a, b, (((0,), (0,)), ((), ())), preferred_element_type=F32)


def _dot_exact(a, b):
    return jnp.dot(a, b, precision=lax.Precision.HIGHEST, preferred_element_type=F32)


def _shift_rows(x, s):
    n = x.shape[0]
    rows = lax.broadcasted_iota(jnp.int32, x.shape, 0)
    y = pltpu.roll(x, s % n, 0)
    if s > 0:
        return jnp.where(rows >= s, y, 0.0)
    return jnp.where(rows < n + s, y, 0.0)


def _rms_bwd(dh, xh, r, g):
    dxh = dh * g
    return r * (dxh - xh * jnp.mean(dxh * xh, axis=-1, keepdims=True))


def _matmul(a, b, mode, out_dtype, name, tm=512, tn=1024, tk=2048):
    if mode == "nn":
        (m, k), (k2, n) = a.shape, b.shape
    elif mode == "nt":
        (m, k), (n, k2) = a.shape, b.shape
    else:
        (k, m), (k2, n) = a.shape, b.shape
    assert k == k2
    tm, tn, tk = min(tm, m), min(tn, n), min(tk, k)
    assert m % tm == 0 and n % tn == 0 and k % tk == 0
    nk = k // tk
    dot = {"nn": _dot, "nt": _dot_nt, "tn": _dot_tn}[mode]

    def body(a_ref, b_ref, o_ref, acc):
        kk = pl.program_id(2)
        part = dot(a_ref[...].astype(BF16), b_ref[...].astype(BF16))
        if nk == 1:
            o_ref[...] = part.astype(out_dtype)
            return

        @pl.when(kk == 0)
        def _():
            acc[...] = part

        @pl.when(kk > 0)
        def _():
            acc[...] += part

        @pl.when(kk == nk - 1)
        def _():
            o_ref[...] = acc[...].astype(out_dtype)

    if mode == "tn":
        a_spec = pl.BlockSpec((tk, tm), lambda i, j, kk: (kk, i))
    else:
        a_spec = pl.BlockSpec((tm, tk), lambda i, j, kk: (i, kk))
    if mode == "nt":
        b_spec = pl.BlockSpec((tn, tk), lambda i, j, kk: (j, kk))
    else:
        b_spec = pl.BlockSpec((tk, tn), lambda i, j, kk: (kk, j))
    return pl.pallas_call(
        body, name=name,
        out_shape=jax.ShapeDtypeStruct((m, n), out_dtype),
        grid=(m // tm, n // tn, nk),
        in_specs=[a_spec, b_spec],
        out_specs=pl.BlockSpec((tm, tn), lambda i, j, kk: (i, j)),
        scratch_shapes=[pltpu.VMEM((tm, tn) if nk > 1 else (8, LANES), F32)],
        compiler_params=_params(dimension_semantics=("parallel", "parallel", "arbitrary")),
    )(a, b)


def _in_proj(x, g1, w_in_t, b_in):
    tm = 512

    def body(x_ref, g_ref, w_hbm, b_ref, qkv_ref, uy_ref, gg_ref, h_ref, w):
        @pl.when(pl.program_id(0) == 0)
        def _():
            pltpu.sync_copy(w_hbm, w)

        xv = x_ref[...]
        r = lax.rsqrt(jnp.mean(xv * xv, axis=-1, keepdims=True) + EPS)
        h = ((xv * r) * g_ref[...]).astype(BF16)
        h_ref[...] = h
        row0 = 0
        for ref in (qkv_ref, uy_ref, gg_ref):
            for c0 in range(0, ref.shape[1], TILE):
                z = _dot_nt(h, w[row0:row0 + TILE, :]) + b_ref[:, row0:row0 + TILE]
                ref[:, c0:c0 + TILE] = z.astype(ref.dtype)
                row0 += TILE

    tok = lambda width: pl.BlockSpec((tm, width), lambda i: (i, 0))
    return pl.pallas_call(
        body, name="in_proj",
        out_shape=(jax.ShapeDtypeStruct((T, 3 * D_ATT), BF16),
                   jax.ShapeDtypeStruct((T, 2 * D_REC), F32),
                   jax.ShapeDtypeStruct((T, 2 * D), F32),
                   jax.ShapeDtypeStruct((T, D), BF16)),
        grid=(T // tm,),
        in_specs=[tok(D), pl.BlockSpec((1, D), lambda i: (0, 0)), pl.BlockSpec(memory_space=pl.ANY),
                  pl.BlockSpec((1, D_IN), lambda i: (0, 0))],
        out_specs=(tok(3 * D_ATT), tok(2 * D_REC), tok(2 * D), tok(D)),
        scratch_shapes=[pltpu.VMEM((D_IN, D), BF16)],
        compiler_params=_params(dimension_semantics=("arbitrary",)),
    )(x, g1, w_in_t, b_in)


def _dz_specs(rows, tile_of, row_of):
    def spec(off, n, per_plane):
        def index(*ids):
            t = jnp.clip(tile_of(*ids) - off, 0, n - 1)
            return (t // per_plane, row_of(*ids), t % per_plane)
        return pl.BlockSpec((1, rows, TILE), index)
    return [spec(off, n, per) for off, n, per in DZ_ARRAYS]


def _dh_norm1_bwd(dz, w_in_t, x, g1, dx1):
    tm = 512

    def body(dqkv_ref, duy_ref, dgg_ref, w_hbm, x_ref, g_ref, dx1_ref, gx_ref, dg_ref, w):
        @pl.when(pl.program_id(0) == 0)
        def _():
            pltpu.sync_copy(w_hbm, w)
            dg_ref[...] = jnp.zeros_like(dg_ref)

        dh, row0 = None, 0
        for ref in (dqkv_ref, duy_ref, dgg_ref):
            for plane in range(ref.shape[0]):
                cols = ref.shape[2]
                part = _dot(ref[plane], w[row0:row0 + cols, :])
                dh = part if dh is None else dh + part
                row0 += cols
        xv = x_ref[...]
        r = lax.rsqrt(jnp.mean(xv * xv, axis=-1, keepdims=True) + EPS)
        xh = xv * r
        dg_ref[...] += jnp.sum(dh * xh, axis=0, keepdims=True)
        gx_ref[...] = dx1_ref[...] + _rms_bwd(dh, xh, r, g_ref[...])

    tok = pl.BlockSpec((tm, D), lambda i: (i, 0))
    vec = pl.BlockSpec((1, D), lambda i: (0, 0))
    planes = lambda a: pl.BlockSpec((a.shape[0], tm, a.shape[2]), lambda i: (0, i, 0))
    return pl.pallas_call(
        body, name="dh_norm1_bwd",
        out_shape=(jax.ShapeDtypeStruct((T, D), F32), jax.ShapeDtypeStruct((1, D), F32)),
        grid=(T // tm,),
        in_specs=[planes(a) for a in dz] + [pl.BlockSpec(memory_space=pl.ANY), tok, vec, tok],
        out_specs=(tok, vec),
        scratch_shapes=[pltpu.VMEM((D_IN, D), BF16)],
        compiler_params=_params(dimension_semantics=("arbitrary",)),
    )(*dz, w_in_t, x, g1, dx1)


def _grad_w_in(dz, h):
    def body(*refs):
        seg_refs = refs[:3]
        h_ref, gw_ref, gb_ref = refs[3:]
        j = pl.program_id(0)

        for s, (off, n, _) in enumerate(DZ_ARRAYS):
            @pl.when((j >= off) & (j < off + n))
            def _(s=s):
                a = seg_refs[s][0]
                gw_ref[...] = _dot_tn(a, h_ref[...]).astype(BF16)
                gb_ref[...] = jnp.sum(a.astype(F32), axis=0, keepdims=True)

    return pl.pallas_call(
        body, name="grad_w_in",
        out_shape=(jax.ShapeDtypeStruct((D_IN, D), BF16), jax.ShapeDtypeStruct((1, D_IN), F32)),
        grid=(N_DZ_TILES,),
        in_specs=_dz_specs(T, lambda j: j, lambda j: 0) + [pl.BlockSpec((T, D), lambda j: (0, 0))],
        out_specs=(pl.BlockSpec((TILE, D), lambda j: (j, 0)), pl.BlockSpec((1, TILE), lambda j: (0, j))),
        compiler_params=_params(dimension_semantics=("parallel",)),
    )(*dz, h)


def _rpb_rows(rpb):
    padded = jnp.pad(rpb, ((0, 0), (0, 0), (0, GRID_W - N_RPB_C)))
    rows = [padded[:, WIN_H - 1 - oi: 2 * WIN_H - 1 - oi].reshape(N_HEADS // HG, HG, KWIN)
            for oi in range(WIN_H)]
    return jnp.stack(rows, axis=0)


SKEW = KWIN - (WIN_W - 1)


MASKED = -1e30


def _bias_tiles(rows_ref, valid, bias_s):
    for oi in range(WIN_H):
        for hh in range(HG):
            row = jnp.broadcast_to(rows_ref[oi, 0, hh:hh + 1, :], (GRID_W, KWIN))
            tile = pltpu.roll(row, SKEW, 1, stride=1, stride_axis=0)
            bias_s[oi, hh * GRID_W:(hh + 1) * GRID_W, :] = jnp.where(valid, tile, MASKED)


def _bias_tile_grads(gb_s, flip, out_ref):
    for oi in range(WIN_H):
        for hh in range(HG):
            g = _dot_exact(flip, gb_s[oi, hh * GRID_W:(hh + 1) * GRID_W, :])
            back = pltpu.roll(g, KWIN - (GRID_W - WIN_W), 1, stride=1, stride_axis=0)
            out_ref[0, oi, hh:hh + 1, :] = jnp.sum(back, axis=0, keepdims=True)


def _rpb_fold(row_grads):
    g = row_grads.transpose(1, 0, 2, 3).reshape(WIN_H, N_HEADS, WIN_H, GRID_W)
    g = g.transpose(0, 2, 1, 3)

    def body(g_ref, o_ref):
        for dr in range(N_RPB_R):
            terms = [g_ref[oi, i] for oi in range(WIN_H) for i in range(WIN_H) if i - oi + WIN_H - 1 == dr]
            acc = terms[0]
            for term in terms[1:]:
                acc = acc + term
            o_ref[dr] = acc

    out = pl.pallas_call(
        body, name="rpb_fold",
        out_shape=jax.ShapeDtypeStruct((N_RPB_R, N_HEADS, GRID_W), F32),
    )(g)
    return out.transpose(1, 0, 2)[:, :, :N_RPB_C]


ATT_GROUPS = N_HEADS // HG
ATT_UNROLL = 2


def _stacked(rows64, same_head):
    return jnp.where(same_head, jnp.concatenate([rows64] * HG, axis=0), jnp.zeros((), BF16))


def _own_heads(stacked):
    head = lax.broadcasted_iota(jnp.int32, (GRID_W, HC), 1) // DH
    out = stacked[:GRID_W]
    for h in range(1, HG):
        out = jnp.where(head == h, stacked[h * GRID_W:(h + 1) * GRID_W], out)
    return out


def _att_scores(q_ref, k_ref, bias_ref, same_head, r):
    rs = jnp.clip(r - WIN_H // 2, 0, ROWS - WIN_H)
    oi = r - rs
    q0 = pl.multiple_of(r * GRID_W, GRID_W)
    k0 = pl.multiple_of(rs * GRID_W, GRID_W)
    q2 = _stacked(q_ref[pl.ds(q0, GRID_W), :] * (DH ** -0.5), same_head)
    kw = k_ref[pl.ds(k0, KWIN), :]
    s = _dot_nt(q2, kw) + bias_ref[oi]
    e = jnp.exp(s - jnp.max(s, axis=-1, keepdims=True))
    return e, 1.0 / jnp.sum(e, axis=-1, keepdims=True), q2, kw, q0, k0, oi


def _att_specs():
    col = lambda off: pl.BlockSpec((T, HC), lambda g: (0, g + off * ATT_GROUPS))
    tables = [pl.BlockSpec((WIN_H, 1, HG, KWIN), lambda g: (0, g, 0, 0)),
              pl.BlockSpec((GRID_W, KWIN), lambda g: (0, 0)),
              pl.BlockSpec((HQ, HC), lambda g: (0, 0))]
    return col, tables, pltpu.VMEM((WIN_H, HQ, KWIN), F32)


def _att_fwd(qkv, bias_rows):
    valid_np, same_head_np = _att_tables()

    def body(q_ref, k_ref, v_ref, rows_ref, valid_ref, head_ref, o_ref, bias_s):
        same_head = head_ref[...] > 0.5
        _bias_tiles(rows_ref, valid_ref[...] > 0.5, bias_s)

        def row(r, carry):
            e, rl, _, _, q0, k0, _ = _att_scores(q_ref, k_ref, bias_s, same_head, r)
            o2 = _dot((e * rl).astype(BF16), v_ref[pl.ds(k0, KWIN), :])
            o_ref[pl.ds(q0, GRID_W), :] = _own_heads(o2).astype(BF16)
            return carry

        lax.fori_loop(0, ROWS, row, 0, unroll=ATT_UNROLL)

    col, tables, tiles = _att_specs()
    return pl.pallas_call(
        body, name="att_fwd",
        out_shape=jax.ShapeDtypeStruct((T, D_ATT), BF16),
        grid=(ATT_GROUPS,),
        in_specs=[col(0), col(1), col(2)] + tables,
        out_specs=col(0),
        scratch_shapes=[tiles],
        compiler_params=_params(dimension_semantics=("parallel",)),
    )(qkv, qkv, qkv, bias_rows, jnp.asarray(valid_np), jnp.asarray(same_head_np))


def _att_bwd(qkv, bias_rows, datt, after):
    valid_np, same_head_np = _att_tables()

    def body(q_ref, k_ref, v_ref, do_ref, rows_ref, valid_ref, head_ref, flip_ref,
             dqkv_ref, grows_ref, dk_acc, dv_acc, bias_s, gb_s):
        same_head = head_ref[...] > 0.5
        dk_acc[...] = jnp.zeros_like(dk_acc)
        dv_acc[...] = jnp.zeros_like(dv_acc)
        gb_s[...] = jnp.zeros_like(gb_s)
        _bias_tiles(rows_ref, valid_ref[...] > 0.5, bias_s)

        def row(r, carry):
            e, rl, q2, kw, q0, k0, oi = _att_scores(q_ref, k_ref, bias_s, same_head, r)
            do2 = _stacked(do_ref[pl.ds(q0, GRID_W), :], same_head)
            vw = v_ref[pl.ds(k0, KWIN), :]
            p = e * rl
            dp = _dot_nt(do2, vw)
            ds = p * (dp - jnp.sum(dp * p, axis=-1, keepdims=True))
            p16 = p.astype(BF16)
            ds16 = ds.astype(BF16)
            dv_acc[pl.ds(k0, KWIN), :] += _dot_tn(p16, do2)
            dk_acc[pl.ds(k0, KWIN), :] += _dot_tn(ds16, q2)
            dq2 = _dot(ds16, kw) * (DH ** -0.5)
            dqkv_ref[0, pl.ds(q0, GRID_W), :] = _own_heads(dq2).astype(BF16)
            gb_s[oi] += ds
            return carry

        lax.fori_loop(0, ROWS, row, 0, unroll=ATT_UNROLL)
        dqkv_ref[1] = dk_acc[...].astype(BF16)
        dqkv_ref[2] = dv_acc[...].astype(BF16)
        _bias_tile_grads(gb_s, flip_ref[...], grows_ref)

    col, tables, tiles = _att_specs()
    return pl.pallas_call(
        body, name="att_bwd",
        out_shape=(jax.ShapeDtypeStruct((3, T, D_ATT), BF16),
                   jax.ShapeDtypeStruct((ATT_GROUPS, WIN_H, HG, KWIN), F32)),
        grid=(ATT_GROUPS,),
        in_specs=[col(0), col(1), col(2), col(0)] + tables + [pl.BlockSpec((GRID_W, GRID_W), lambda g: (0, 0))],
        out_specs=(pl.BlockSpec((3, T, HC), lambda g: (0, 0, g)),
                   pl.BlockSpec((1, WIN_H, HG, KWIN), lambda g: (g, 0, 0, 0))),
        scratch_shapes=[pltpu.VMEM((T, HC), F32), pltpu.VMEM((T, HC), F32), tiles, tiles],
        compiler_params=_params(dimension_semantics=("parallel",)),
    )(qkv, qkv, qkv, datt, bias_rows, jnp.asarray(valid_np) + after, jnp.asarray(same_head_np),
      jnp.asarray(np.eye(GRID_W, dtype=np.float32)[::-1].copy()))


def _conv_taps(up):
    return (_shift_rows(up, 2), _shift_rows(up, 1), up, _shift_rows(up, -1))


def _pair_block_diag(w_pair, dup, same_half):
    return jnp.where(same_half, _dot(w_pair.astype(BF16), dup), 0.0).astype(BF16)


def _gates(u, u16, wa, ba, wi, bi, lam):
    r = _sigmoid(_dot(u16, wa) + ba)
    ig = _sigmoid(_dot(u16, wi) + bi)
    sp = _softplus(-lam)
    log_a = (-LRU_C) * r * sp
    a = jnp.exp(log_a)
    mult2 = jnp.maximum(_one_minus_square(log_a, a), 0.0)
    return r, ig, sp, a, jnp.sqrt(mult2), mult2


SCAN_BLOCKS = 2


def _scans(jobs):
    c = jobs[0][0].shape[1]
    nblk = T // 8
    rows = lax.broadcasted_iota(jnp.int32, (8, c), 0)

    def block(a, b, reverse):
        for s in (1, 2, 4):
            if reverse:
                keep = rows < 8 - s
                a_s = jnp.where(keep, pltpu.roll(a, 8 - s, 0), 1.0)
                b_s = jnp.where(keep, pltpu.roll(b, 8 - s, 0), 0.0)
            else:
                keep = rows >= s
                a_s = jnp.where(keep, pltpu.roll(a, s, 0), 1.0)
                b_s = jnp.where(keep, pltpu.roll(b, s, 0), 0.0)
            b = a * b_s + b
            a = a * a_s
        return a, b

    def step(i, carry):
        out = []
        for (a_ref, b_ref, h_ref, reverse), h_prev in zip(jobs, carry):
            for u in range(SCAN_BLOCKS):
                blk = i * SCAN_BLOCKS + u
                if reverse:
                    blk = nblk - 1 - blk
                t0 = pl.multiple_of(blk * 8, 8)
                a, b = block(a_ref[pl.ds(t0, 8), :], b_ref[pl.ds(t0, 8), :], reverse)
                h = a * h_prev + b
                h_ref[pl.ds(t0, 8), :] = h
                h_prev = jnp.broadcast_to(h[0:1] if reverse else h[7:8], (8, c))
            out.append(h_prev)
        return tuple(out)

    lax.fori_loop(0, nblk // SCAN_BLOCKS, step, tuple(jnp.zeros((8, c), F32) for _ in jobs))


def _rec_specs():
    tok = lambda off: pl.BlockSpec((T, CG), lambda g: (0, g + off))
    per_ch = lambda rows: pl.BlockSpec((rows, CG), lambda g: (0, g))
    wspec = pl.BlockSpec((2, 1, CG, REC_BLOCK), lambda g: (0, g, 0, 0))
    const = lambda shape: pl.BlockSpec(shape, lambda g: (0, 0))
    return tok, per_ch, wspec, const


def _rec_fwd(uy, conv_w, conv_b, w_a, b_a, w_i, b_i, lam):
    tok, per_ch, wspec, const = _rec_specs()

    def body(up_ref, yb_ref, cw_ref, cb_ref, wa_ref, ba_ref, wi_ref, bi_ref, lam_ref, dup_ref, half_ref,
             hf_ref, hb_ref, yrec_ref, am_ref, bx_f, bx_b):
        dup = dup_ref[...]
        same_half = half_ref[...] > 0.5
        taps = _conv_taps(up_ref[...])
        u = cb_ref[...]
        for j in range(4):
            u = u + taps[j] * cw_ref[j:j + 1, :]
        u16 = u.astype(BF16)
        for d, bx_s in enumerate((bx_f, bx_b)):
            wa = _pair_block_diag(wa_ref[d, 0], dup, same_half)
            wi = _pair_block_diag(wi_ref[d, 0], dup, same_half)
            _, ig, _, a, mult, _ = _gates(u, u16, wa, ba_ref[d:d + 1, :], wi, bi_ref[d:d + 1, :],
                                       lam_ref[d:d + 1, :])
            am_ref[2 * d] = a
            am_ref[2 * d + 1] = mult
            bx_s[...] = mult * (ig * u)
        _scans([(am_ref.at[0], bx_f, hf_ref, False), (am_ref.at[2], bx_b, hb_ref, True)])
        gelu, _ = _gelu_and_grad(yb_ref[...])
        yrec_ref[...] = ((hf_ref[...] + hb_ref[...]) * gelu).astype(BF16)

    return pl.pallas_call(
        body, name="rec_fwd",
        out_shape=(jax.ShapeDtypeStruct((T, D_REC), F32), jax.ShapeDtypeStruct((T, D_REC), F32),
                   jax.ShapeDtypeStruct((T, D_REC), BF16), jax.ShapeDtypeStruct((4, T, D_REC), F32)),
        grid=(N_CG,),
        in_specs=[tok(0), tok(N_CG), per_ch(4), per_ch(1), wspec, per_ch(2), wspec, per_ch(2), per_ch(2),
                  const((REC_BLOCK, CG)), const((CG, CG))],
        out_specs=(tok(0), tok(0), tok(0), pl.BlockSpec((4, T, CG), lambda g: (0, 0, g))),
        scratch_shapes=[pltpu.VMEM((T, CG), F32)] * 2,
        compiler_params=_params(dimension_semantics=("parallel",)),
    )(uy, uy, conv_w, conv_b, w_a, b_a, w_i, b_i, lam,
      jnp.asarray(_dup_table(), BF16), jnp.asarray(_pair_mask()))


def _rec_bwd(uy, hf, hb, am, dyrec, conv_w, conv_b, w_a, b_a, w_i, b_i, lam):
    tok, per_ch, wspec, const = _rec_specs()

    def body(up_ref, yb_ref, hf_ref, hb_ref, am_ref, dy_ref, cw_ref, cb_ref, wa_ref, ba_ref, wi_ref, bi_ref,
             lam_ref, dup_ref, dupt_ref, half_ref,
             duy_ref, dcw_ref, dcb_ref, dwa_ref, dba_ref, dwi_ref, dbi_ref, dlam_ref,
             a_s0, a_s1, dh_s, g_s0, g_s1):
        dup = dup_ref[...]
        dup_t = dupt_ref[...]
        same_half = half_ref[...] > 0.5
        taps = _conv_taps(up_ref[...])
        u = cb_ref[...]
        for j in range(4):
            u = u + taps[j] * cw_ref[j:j + 1, :]
        u16 = u.astype(BF16)
        gelu, dgelu = _gelu_and_grad(yb_ref[...])
        dy = dy_ref[...]
        duy_ref[1] = (dy * (hf_ref[...] + hb_ref[...]) * dgelu).astype(BF16)
        dh_s[...] = dy * gelu
        a_s0[...] = _shift_rows(am_ref[0], -1)
        a_s1[...] = _shift_rows(am_ref[2], 1)
        _scans([(a_s0, dh_s, g_s0, True), (a_s1, dh_s, g_s1, False)])
        du = jnp.zeros((T, CG), F32)
        for d, g_s in enumerate((g_s0, g_s1)):
            reverse = d == 1
            wa = _pair_block_diag(wa_ref[d, 0], dup, same_half)
            wi = _pair_block_diag(wi_ref[d, 0], dup, same_half)
            lam_d = lam_ref[d:d + 1, :]
            r = _sigmoid(_dot(u16, wa) + ba_ref[d:d + 1, :])
            ig = _sigmoid(_dot(u16, wi) + bi_ref[d:d + 1, :])
            sp = _softplus(-lam_d)
            a, mult = am_ref[2 * d], am_ref[2 * d + 1]
            mult2 = mult * mult
            g = g_s[...]
            h_prev = _shift_rows(hb_ref[...], -1) if reverse else _shift_rows(hf_ref[...], 1)
            da = g * h_prev
            dmult = g * (ig * u)
            dig = g * mult * u
            du = du + g * mult * ig
            dmult_dlog = jnp.where(mult2 > 0.0, -(a * a) * lax.rsqrt(mult2), 0.0)
            dlog_a = da * a + dmult * dmult_dlog
            dr = dlog_a * ((-LRU_C) * sp)
            dsp = jnp.sum(dlog_a * ((-LRU_C) * r), axis=0, keepdims=True)
            dlam_ref[d:d + 1, :] = dsp * (-_sigmoid(-lam_d))
            dga = dr * r * (1.0 - r)
            dgi = dig * ig * (1.0 - ig)
            dga16 = dga.astype(BF16)
            dgi16 = dgi.astype(BF16)
            du = du + _dot_nt(dga16, wa) + _dot_nt(dgi16, wi)
            dwa_ref[d, 0] = _dot_exact(jnp.where(same_half, _dot_tn(u16, dga16), 0.0), dup_t)
            dwi_ref[d, 0] = _dot_exact(jnp.where(same_half, _dot_tn(u16, dgi16), 0.0), dup_t)
            dba_ref[d:d + 1, :] = jnp.sum(dga, axis=0, keepdims=True)
            dbi_ref[d:d + 1, :] = jnp.sum(dgi, axis=0, keepdims=True)
        dcb_ref[...] = jnp.sum(du, axis=0, keepdims=True)
        for j in range(4):
            dcw_ref[j:j + 1, :] = jnp.sum(du * taps[j], axis=0, keepdims=True)
        dup_in = (_shift_rows(du, -2) * cw_ref[0:1, :] + _shift_rows(du, -1) * cw_ref[1:2, :]
                  + du * cw_ref[2:3, :] + _shift_rows(du, 1) * cw_ref[3:4, :])
        duy_ref[0] = dup_in.astype(BF16)

    wshape = jax.ShapeDtypeStruct((2, N_CG, CG, REC_BLOCK), F32)
    vec = lambda rows: jax.ShapeDtypeStruct((rows, D_REC), F32)
    dup_np = _dup_table()
    return pl.pallas_call(
        body, name="rec_bwd",
        out_shape=(jax.ShapeDtypeStruct((2, T, D_REC), BF16),
                   vec(4), vec(1), wshape, vec(2), wshape, vec(2), vec(2)),
        grid=(N_CG,),
        in_specs=[tok(0), tok(N_CG), tok(0), tok(0), pl.BlockSpec((4, T, CG), lambda g: (0, 0, g)), tok(0),
                  per_ch(4), per_ch(1), wspec, per_ch(2), wspec, per_ch(2), per_ch(2),
                  const((REC_BLOCK, CG)), const((CG, REC_BLOCK)), const((CG, CG))],
        out_specs=(pl.BlockSpec((2, T, CG), lambda g: (0, 0, g)),
                   per_ch(4), per_ch(1), wspec, per_ch(2), wspec, per_ch(2), per_ch(2)),
        scratch_shapes=[pltpu.VMEM((T, CG), F32)] * 5,
        compiler_params=_params(dimension_semantics=("parallel",)),
    )(uy, uy, hf, hb, am, dyrec, conv_w, conv_b, w_a, b_a, w_i, b_i, lam,
      jnp.asarray(dup_np, BF16), jnp.asarray(dup_np.T.copy()), jnp.asarray(_pair_mask()))


TM_MIX = 256


def _mix_specs():
    tok = lambda width, blk=0: pl.BlockSpec((TM_MIX, width), lambda i: (i, blk))
    full = lambda shape: pl.BlockSpec(shape, lambda i: (0, 0))
    return tok, full


def _mix_fwd(x, att, yrec, gg, w_att_o_t, w_rec_o, w_out):
    tok, full = _mix_specs()

    def body(x_ref, att_ref, yr_ref, ga_ref, gr_ref, wao_ref, wro_ref, wo_ref, x1_ref, mixed_ref):
        y_att = _dot_nt(att_ref[...], wao_ref[...])
        y_rec = _dot(yr_ref[...], wro_ref[...])
        mixed = (_sigmoid(ga_ref[...]) * y_att + _sigmoid(gr_ref[...]) * y_rec).astype(BF16)
        mixed_ref[...] = mixed
        x1_ref[...] = x_ref[...] + _dot(mixed, wo_ref[...])

    return pl.pallas_call(
        body, name="mix_fwd",
        out_shape=(jax.ShapeDtypeStruct((T, D), F32), jax.ShapeDtypeStruct((T, D), BF16)),
        grid=(T // TM_MIX,),
        in_specs=[tok(D), tok(D_ATT), tok(D_REC), tok(D, 0), tok(D, 1),
                  full((D, D_ATT)), full((D_REC, D)), full((D, D))],
        out_specs=(tok(D), tok(D)),
        compiler_params=_params(dimension_semantics=("parallel",)),
    )(x, att, yrec, gg, gg, w_att_o_t, w_rec_o, w_out)


def _mix_bwd(dx1, att, yrec, gg, w_att_o_t, w_rec_o, w_out):
    tok, full = _mix_specs()

    def body(dx_ref, att_ref, yr_ref, ga_ref, gr_ref, wao_ref, wro_ref, wo_ref,
             dgg_ref, dya_ref, dyr_ref, datt_ref, dyrp_ref):
        dmixed = _dot_nt(dx_ref[...].astype(BF16), wo_ref[...])
        y_att = _dot_nt(att_ref[...], wao_ref[...])
        y_rec = _dot(yr_ref[...], wro_ref[...])
        sa = _sigmoid(ga_ref[...])
        sr = _sigmoid(gr_ref[...])
        dgg_ref[0] = (dmixed * y_att * sa * (1.0 - sa)).astype(BF16)
        dgg_ref[1] = (dmixed * y_rec * sr * (1.0 - sr)).astype(BF16)
        dya = (dmixed * sa).astype(BF16)
        dyr = (dmixed * sr).astype(BF16)
        dya_ref[...] = dya
        dyr_ref[...] = dyr
        datt_ref[...] = _dot(dya, wao_ref[...]).astype(BF16)
        dyrp_ref[...] = _dot_nt(dyr, wro_ref[...])

    return pl.pallas_call(
        body, name="mix_bwd",
        out_shape=(jax.ShapeDtypeStruct((2, T, D), BF16),
                   jax.ShapeDtypeStruct((T, D), BF16), jax.ShapeDtypeStruct((T, D), BF16),
                   jax.ShapeDtypeStruct((T, D_ATT), BF16), jax.ShapeDtypeStruct((T, D_REC), F32)),
        grid=(T // TM_MIX,),
        in_specs=[tok(D), tok(D_ATT), tok(D_REC), tok(D, 0), tok(D, 1),
                  full((D, D_ATT)), full((D_REC, D)), full((D, D))],
        out_specs=(pl.BlockSpec((2, TM_MIX, D), lambda i: (0, i, 0)),
                   tok(D), tok(D), tok(D_ATT), tok(D_REC)),
        compiler_params=_params(dimension_semantics=("parallel",)),
    )(dx1, att, yrec, gg, gg, w_att_o_t, w_rec_o, w_out)


TM_FFN = 256
FF_CHUNK = 1024


def _ffn_loss(x1, target, g2, gf, w_ff1_t, w_ff2):
    n_chunks = D_FF // FF_CHUNK

    def body(x1_ref, tg_ref, g2_ref, gf_ref, w1_hbm, w2_hbm,
             loss_ref, dx1_ref, h2_ref, act_ref, dpre_ref, dx2_ref, dg2_ref, dgf_ref,
             w1, w2, relu_s):
        i = pl.program_id(0)

        @pl.when(i == 0)
        def _():
            pltpu.sync_copy(w1_hbm, w1)
            pltpu.sync_copy(w2_hbm, w2)
            loss_ref[...] = jnp.zeros_like(loss_ref)
            dg2_ref[...] = jnp.zeros_like(dg2_ref)
            dgf_ref[...] = jnp.zeros_like(dgf_ref)

        x1v = x1_ref[...]
        r2 = lax.rsqrt(jnp.mean(x1v * x1v, axis=-1, keepdims=True) + EPS)
        xh2 = x1v * r2
        h2 = (xh2 * g2_ref[...]).astype(BF16)
        h2_ref[...] = h2
        x2 = x1v
        for c in range(n_chunks):
            ff = slice(c * FF_CHUNK, (c + 1) * FF_CHUNK)
            rl = jnp.maximum(_dot_nt(h2, w1[ff, :]), 0.0)
            relu_s[:, ff] = rl
            act = (rl * rl).astype(BF16)
            act_ref[:, ff] = act
            x2 = x2 + _dot(act, w2[ff, :])
        r3 = lax.rsqrt(jnp.mean(x2 * x2, axis=-1, keepdims=True) + EPS)
        xh3 = x2 * r3
        err = xh3 * gf_ref[...] - tg_ref[...]
        loss_ref[...] += 0.5 * jnp.sum(jnp.mean(err * err, axis=-1, keepdims=True))
        dy = err * (1.0 / D)
        dgf_ref[...] += jnp.sum(dy * xh3, axis=0, keepdims=True)
        dx2 = _rms_bwd(dy, xh3, r3, gf_ref[...])
        dx2_16 = dx2.astype(BF16)
        dx2_ref[...] = dx2_16
        dh2 = jnp.zeros((TM_FFN, D), F32)
        for c in range(n_chunks):
            ff = slice(c * FF_CHUNK, (c + 1) * FF_CHUNK)
            dpre = (_dot_nt(dx2_16, w2[ff, :]) * (2.0 * relu_s[:, ff])).astype(BF16)
            dpre_ref[:, ff] = dpre
            dh2 = dh2 + _dot(dpre, w1[ff, :])
        dg2_ref[...] += jnp.sum(dh2 * xh2, axis=0, keepdims=True)
        dx1_ref[...] = dx2 + _rms_bwd(dh2, xh2, r2, g2_ref[...])

    tok = lambda width: pl.BlockSpec((TM_FFN, width), lambda i: (i, 0))
    vec = pl.BlockSpec((1, D), lambda i: (0, 0))
    hbm = pl.BlockSpec(memory_space=pl.ANY)
    return pl.pallas_call(
        body, name="ffn_loss",
        out_shape=(jax.ShapeDtypeStruct((8, 128), F32), jax.ShapeDtypeStruct((T, D), F32),
                   jax.ShapeDtypeStruct((T, D), BF16), jax.ShapeDtypeStruct((T, D_FF), BF16),
                   jax.ShapeDtypeStruct((T, D_FF), BF16), jax.ShapeDtypeStruct((T, D), BF16),
                   jax.ShapeDtypeStruct((1, D), F32), jax.ShapeDtypeStruct((1, D), F32)),
        grid=(T // TM_FFN,),
        in_specs=[tok(D), tok(D), vec, vec, hbm, hbm],
        out_specs=(pl.BlockSpec((8, 128), lambda i: (0, 0)), tok(D), tok(D), tok(D_FF), tok(D_FF), tok(D),
                   vec, vec),
        scratch_shapes=[pltpu.VMEM((D_FF, D), BF16), pltpu.VMEM((D_FF, D), BF16),
                        pltpu.VMEM((TM_FFN, D_FF), F32)],
        compiler_params=_params(dimension_semantics=("arbitrary",)),
    )(x1, target, g2, gf, w_ff1_t, w_ff2)


def _local_step(x, target, p, late_weights, reduce_early):
    bias = _rpb_rows(p["rpb"])
    pairs = lambda w: w.reshape(2, N_CG, CG, REC_BLOCK)
    w_a, w_i = pairs(p["w_rg_a"]), pairs(p["w_rg_i"])
    rec_params = (p["conv_w"], p["conv_b"], w_a, p["b_rg_a"], w_i, p["b_rg_i"], p["lru_lambda"])

    qkv, uy, gg, h = _in_proj(x, p["ln1_g"], p["w_in_t"], p["b_in"])
    att = _att_fwd(qkv, bias)
    hf, hb, yrec, am = _rec_fwd(uy, *rec_params)
    p = {**p, **late_weights(yrec)}
    x1, mixed = _mix_fwd(x, att, yrec, gg, p["w_att_o_t"], p["w_rec_o"], p["w_out"])
    loss8, dx1, h2, act, dpre, dx2, g_ln2, g_lnf = _ffn_loss(
        x1, target, p["ln2_g"], p["lnf_g"], p["w_ff1_t"], p["w_ff2"])

    dgg, dya, dyr, datt, dyrp = _mix_bwd(dx1, att, yrec, gg, p["w_att_o_t"], p["w_rec_o"], p["w_out"])
    duy, g_cw, g_cb, g_wa, g_ba, g_wi, g_bi, g_lam = _rec_bwd(uy, hf, hb, am, dyrp, *rec_params)
    blocks = lambda g: g.reshape(2, N_REC_BLOCKS, REC_BLOCK, REC_BLOCK)
    grads = {
        "w_att_o_t": _matmul(dya, att, "tn", BF16, "g_w_att_o"),
        "conv_w": g_cw, "conv_b": g_cb, "w_rg_a": blocks(g_wa), "b_rg_a": g_ba,
        "w_rg_i": blocks(g_wi), "b_rg_i": g_bi, "lru_lambda": g_lam,
        "w_rec_o": _matmul(yrec, dyr, "tn", BF16, "g_w_rec_o"),
        "w_out": _matmul(mixed, dx1, "tn", BF16, "g_w_out"),
        "ln2_g": g_ln2,
        "w_ff1_t": _matmul(dpre, h2, "tn", BF16, "g_w_ff1"),
        "w_ff2": _matmul(act, dx2, "tn", BF16, "g_w_ff2"),
        "lnf_g": g_lnf,
    }
    dqkv, gbias = _att_bwd(qkv, bias, datt, reduce_early(grads))
    dz = (dqkv, duy, dgg)
    grad_x, g_ln1 = _dh_norm1_bwd(dz, p["w_in_t"], x, p["ln1_g"], dx1)
    g_w_in_t, g_b_in = _grad_w_in(dz, h)
    grads.update(ln1_g=g_ln1, w_in_t=g_w_in_t, b_in=g_b_in, rpb=_rpb_fold(gbias))
    return loss8[0:1, 0:1], grad_x, grads


MESH_ID = pl.DeviceIdType.MESH
ANY = pl.BlockSpec(memory_space=pl.ANY)

CHAN_BLOCK_ROWS = 32
GATE_ROWS = 2 * 2 * N_REC_BLOCKS * REC_BLOCK * REC_BLOCK // (N_DEV * D)
SECTIONS = (("w_in_t", 704, D), ("w_rec_o", 128, D), ("w_out", 128, D), ("w_ff1_t", 512, D),
            ("w_ff2", 512, D), ("chan", CHAN_BLOCK_ROWS, D), ("w_att_o_t", 128, D_ATT),
            ("gates", GATE_ROWS, D))
N_SEC = len(SECTIONS)
N_CHAN_ROWS = 10
CHAN = (("conv_w", 4), ("b_rg_a", 2), ("b_rg_i", 2), ("lru_lambda", 2))


def _position():
    return lax.axis_index("x"), lax.axis_index("y"), lax.axis_index("c")


def _other_chips(x, y):
    return [(1 - x, y), (x, 1 - y), (1 - x, 1 - y)]


PASS_ON_ID, PAIR_EARLY_ID, PAIR_LATE_ID = 1, 2, 3


def _pair_handshake(x, y, c):
    barrier = pltpu.get_barrier_semaphore()
    pl.semaphore_signal(barrier, inc=1, device_id=(x, y, 1 - c), device_id_type=MESH_ID)
    pl.semaphore_wait(barrier, 1)


def _block_of(ref, dev, rows):
    return ref.at[pl.ds(pl.multiple_of(dev * rows, 16), rows)]


def _all_gather(shards, name):
    ns = len(shards)

    def body(*refs):
        x_refs, out_refs, done_ref = refs[:ns], refs[ns:2 * ns], refs[2 * ns]
        send_sems, recv_sems, local_sems = refs[2 * ns + 1:]
        done_ref[0, 0] = 0.0
        x, y, c = _position()
        me, sibling = (x, y, c), (x, y, 1 - c)
        x_nbr, y_nbr, diagonal = _other_chips(x, y)
        north = c == 1
        relay_from = (jnp.where(north, x_nbr[0], y_nbr[0]), jnp.where(north, x_nbr[1], y_nbr[1]))
        relay_to = (jnp.where(north, y_nbr[0], x_nbr[0]), jnp.where(north, y_nbr[1], x_nbr[1]))

        def rows(s, px, py, pc):
            return _block_of(out_refs[s], 4 * px + 2 * py + pc, shards[s].shape[0])

        def copy(k, s, block, to, from_shard=False):
            return pltpu.make_async_remote_copy(
                src_ref=x_refs[s] if from_shard else rows(s, *block), dst_ref=rows(s, *block),
                send_sem=send_sems.at[k * ns + s], recv_sem=recv_sems.at[k * ns + s],
                device_id=to, device_id_type=MESH_ID)

        sections = range(ns)
        mine = [pltpu.make_async_copy(x_refs[s], rows(s, *me), local_sems.at[s]) for s in sections]
        sent = [copy(k, s, me, to, True) for k, to in enumerate((sibling, (*x_nbr, c), (*y_nbr, c)))
                for s in sections]
        for cp in mine + sent:
            cp.start()
        for s in sections:
            copy(1, s, (*x_nbr, c), me).wait_recv()
            copy(2, s, (*y_nbr, c), me).wait_recv()
            sent += [copy(3, s, (*relay_from, c), (*relay_to, c)),
                     copy(4, s, (*x_nbr, c), sibling), copy(5, s, (*y_nbr, c), sibling)]
            for cp in sent[-3:]:
                cp.start()
        for s in sections:
            copy(3, s, (*diagonal, c), me).wait_recv()
            sent.append(copy(6, s, (*diagonal, c), sibling))
            sent[-1].start()
        for s in sections:
            copy(0, s, sibling, me).wait_recv()
            for k, chip in ((4, x_nbr), (5, y_nbr), (6, diagonal)):
                copy(k, s, (*chip, 1 - c), me).wait_recv()
        for cp in sent:
            cp.wait_send()
        for cp in mine:
            cp.wait()

    return pl.pallas_call(
        body, name=name,
        out_shape=tuple(jax.ShapeDtypeStruct((N_DEV * s.shape[0], s.shape[1]), s.dtype) for s in shards)
        + (jax.ShapeDtypeStruct((1, 1), F32),),
        in_specs=[ANY] * ns,
        out_specs=(ANY,) * ns + (pl.BlockSpec(memory_space=pltpu.SMEM),),
        scratch_shapes=[pltpu.SemaphoreType.DMA((7 * ns,)), pltpu.SemaphoreType.DMA((7 * ns,)),
                        pltpu.SemaphoreType.DMA((ns,))],
    )(*shards)


HBM = pl.BlockSpec(memory_space=pltpu.HBM)
SEM = pl.BlockSpec(memory_space=pltpu.SEMAPHORE)
EFFECT = pltpu.SideEffectType.DATAFLOW_SIDE_EFFECTING


def _in_hbm(a):
    return pltpu.with_memory_space_constraint(a, pltpu.HBM)


def _first_hop_copies(shards, x_refs, zones, send_sems, recv_sems):
    ns = len(shards)
    x, y, c = _position()
    targets = [(x, y, 1 - c)] + [(cx, cy, c) for cx, cy in _other_chips(x, y)]
    return [pltpu.make_async_remote_copy(
        src_ref=x_refs[s], dst_ref=_block_of(zones[s], 4 * x + 2 * y + c, shards[s].shape[0]),
        send_sem=send_sems.at[k * ns + s], recv_sem=recv_sems.at[k * ns + s],
        device_id=to, device_id_type=MESH_ID)
        for k, to in enumerate(targets) for s in range(ns)]


def _after_all(arrays, name):
    def body(*refs):
        refs[-1][...] = jnp.zeros_like(refs[-1])

    return pl.pallas_call(
        body, name=name,
        out_shape=jax.ShapeDtypeStruct((8, LANES), F32),
        in_specs=[pl.BlockSpec(memory_space=pl.ANY)] * len(arrays),
        out_specs=pl.BlockSpec(memory_space=pltpu.VMEM),
    )(*arrays)


def _own_blocks_placed(shards, after):
    ns = len(shards)
    x, y, c = _position()
    me = jnp.reshape(4 * x + 2 * y + c, (1,)).astype(jnp.int32)
    shards = [*shards[:-1], shards[-1] + after.astype(shards[-1].dtype)]

    def body(me_ref, *refs):
        for s in range(ns):
            refs[ns + s][...] = refs[s][...]

    return pl.pallas_call(
        body, name="own_blocks_placed",
        out_shape=tuple(jax.ShapeDtypeStruct((N_DEV * s.shape[0], s.shape[1]), s.dtype) for s in shards),
        grid_spec=pltpu.PrefetchScalarGridSpec(
            num_scalar_prefetch=1, grid=(1,),
            in_specs=[pl.BlockSpec(s.shape, lambda i, me: (0, 0)) for s in shards],
            out_specs=tuple(pl.BlockSpec(s.shape, lambda i, me: (me[0], 0)) for s in shards)),
        compiler_params=_params(dimension_semantics=("arbitrary",)),
    )(me, *shards)


def _gather_start(shards, after, name):
    ns = len(shards)
    zones = _own_blocks_placed(shards, after)

    def body(*refs):
        for cp in _first_hop_copies(shards, refs[:ns], refs[ns:2 * ns], refs[2 * ns], refs[2 * ns + 1]):
            cp.start()
        refs[-1][...] = jnp.zeros_like(refs[-1])

    out = pl.pallas_call(
        body, name=name,
        out_shape=(pltpu.SemaphoreType.DMA((4 * ns,)), pltpu.SemaphoreType.DMA((4 * ns,)),
                   *[pltpu.HBM(a.shape, a.dtype) for a in (*shards, *zones)],
                   jax.ShapeDtypeStruct((8, LANES), F32)),
        in_specs=[HBM] * (2 * ns),
        out_specs=(SEM, SEM, *[HBM] * (2 * ns), pl.BlockSpec(memory_space=pltpu.VMEM)),
        input_output_aliases={i: 2 + i for i in range(2 * ns)},
        compiler_params=pltpu.CompilerParams(has_side_effects=EFFECT),
    )(*[_in_hbm(a) for a in shards], *[_in_hbm(a) for a in zones])
    return out[0], out[1], out[2:2 + ns], out[2 + ns:2 + 2 * ns], out[-1]


def _gather_wait(send_sems, recv_sems, shards, zones, after, name):
    ns = len(shards)

    def body(*refs):
        for cp in _first_hop_copies(shards, refs[:ns], refs[ns:2 * ns], refs[2 * ns], refs[2 * ns + 1]):
            cp.wait_send()
            cp.wait_recv()

    out = pl.pallas_call(
        body, name=name,
        out_shape=tuple(pltpu.HBM(a.shape, a.dtype) for a in (*shards, *zones)),
        in_specs=[HBM] * (2 * ns) + [SEM, SEM, ANY],
        out_specs=(HBM,) * (2 * ns),
        input_output_aliases={i: i for i in range(2 * ns)},
        compiler_params=pltpu.CompilerParams(has_side_effects=EFFECT),
    )(*shards, *zones, send_sems, recv_sems, after)
    return out[ns:]


def _gather_pass_on(rows, zones, name):
    ns = len(zones)

    def body(*refs):
        in_refs, out_refs = refs[:ns], refs[ns:2 * ns]
        send_sems, recv_sems = refs[2 * ns:]
        x, y, c = _position()
        _pair_handshake(x, y, c)
        copies = [pltpu.make_async_remote_copy(
            src_ref=_block_of(in_refs[s], 4 * cx + 2 * cy + c, rows[s]),
            dst_ref=_block_of(out_refs[s], 4 * cx + 2 * cy + c, rows[s]),
            send_sem=send_sems.at[j * ns + s], recv_sem=recv_sems.at[j * ns + s],
            device_id=(x, y, 1 - c), device_id_type=MESH_ID)
            for j, (cx, cy) in enumerate(_other_chips(x, y)) for s in range(ns)]
        for cp in copies:
            cp.start()
        for cp in copies:
            cp.wait_recv()
        for cp in copies:
            cp.wait_send()

    return pl.pallas_call(
        body, name=name,
        out_shape=tuple(jax.ShapeDtypeStruct(z.shape, z.dtype) for z in zones),
        in_specs=[ANY] * ns, out_specs=(ANY,) * ns,
        input_output_aliases={i: i for i in range(ns)},
        scratch_shapes=[pltpu.SemaphoreType.DMA((3 * ns,)), pltpu.SemaphoreType.DMA((3 * ns,))],
        compiler_params=pltpu.CompilerParams(collective_id=PASS_ON_ID),
    )(*zones)


def _pair_copies(sections, g_refs, land, send_sems, recv_sems):
    ns = len(sections)
    x, y, c = _position()
    return [pltpu.make_async_remote_copy(
        src_ref=_block_of(g_refs[s], 2 * k + 1 - c, rows), dst_ref=land[s].at[k],
        send_sem=send_sems.at[k * ns + s], recv_sem=recv_sems.at[k * ns + s],
        device_id=(x, y, 1 - c), device_id_type=MESH_ID)
        for k in range(N_CHIPS) for s, (_, rows, _) in enumerate(sections)]


def _pair_exchange_start(sections, grads, barrier_id, name):
    ns = len(sections)

    def body(*refs):
        _pair_handshake(*_position())
        for cp in _pair_copies(sections, refs[:ns], refs[ns:2 * ns], refs[2 * ns], refs[2 * ns + 1]):
            cp.start()
        refs[-1][...] = jnp.zeros_like(refs[-1])

    zones = [lax.empty((N_CHIPS, rows, cols), BF16) for _, rows, cols in sections]
    n = N_CHIPS * ns
    out = pl.pallas_call(
        body, name=name,
        out_shape=(pltpu.SemaphoreType.DMA((n,)), pltpu.SemaphoreType.DMA((n,)),
                   *[pltpu.HBM(a.shape, a.dtype) for a in (*grads, *zones)],
                   jax.ShapeDtypeStruct((8, LANES), F32)),
        in_specs=[HBM] * (2 * ns),
        out_specs=(SEM, SEM, *[HBM] * (2 * ns), pl.BlockSpec(memory_space=pltpu.VMEM)),
        input_output_aliases={i: 2 + i for i in range(2 * ns)},
        compiler_params=pltpu.CompilerParams(has_side_effects=EFFECT, collective_id=barrier_id),
    )(*[_in_hbm(a) for a in grads], *[_in_hbm(a) for a in zones])
    return out[0], out[1], out[2:2 + ns], out[2 + ns:2 + 2 * ns], out[-1]


def _pair_exchange_wait(sections, send_sems, recv_sems, grads, zones, after, name):
    ns = len(sections)

    def body(*refs):
        for cp in _pair_copies(sections, refs[:ns], refs[ns:2 * ns], refs[2 * ns], refs[2 * ns + 1]):
            cp.wait_send()
            cp.wait_recv()

    out = pl.pallas_call(
        body, name=name,
        out_shape=tuple(pltpu.HBM(a.shape, a.dtype) for a in (*grads, *zones)),
        in_specs=[HBM] * (2 * ns) + [SEM, SEM, ANY],
        out_specs=(HBM,) * (2 * ns),
        input_output_aliases={i: i for i in range(2 * ns)},
        compiler_params=pltpu.CompilerParams(has_side_effects=EFFECT),
    )(*grads, *zones, send_sems, recv_sems, after)
    return out[:ns], out[ns:]


def _pair_add(sections, grads, got, core, name):
    ns = len(sections)

    def body(core_ref, *refs):
        g_refs, got_refs, p_refs = refs[:ns], refs[ns:2 * ns], refs[2 * ns:]
        for s in range(ns):
            p_refs[s][0] = (g_refs[s][...].astype(F32) + got_refs[s][0].astype(F32)).astype(BF16)

    slot = [pl.BlockSpec((1, rows, cols), lambda k, c: (k, 0, 0)) for _, rows, cols in sections]
    return pl.pallas_call(
        body, name=name,
        out_shape=tuple(jax.ShapeDtypeStruct((N_CHIPS, rows, cols), BF16) for _, rows, cols in sections),
        grid_spec=pltpu.PrefetchScalarGridSpec(
            num_scalar_prefetch=1, grid=(N_CHIPS,),
            in_specs=[pl.BlockSpec((rows, cols), lambda k, c: (2 * k + c[0], 0)) for _, rows, cols in sections]
            + slot,
            out_specs=tuple(slot)),
        compiler_params=_params(dimension_semantics=("parallel",)),
    )(core, *grads, *got)


def _chip_copies(sections, p_refs, land, send_sems, recv_sems):
    ns = len(sections)
    x, y, c = _position()
    return [pltpu.make_async_remote_copy(
        src_ref=p_refs[s].at[2 * cx + cy], dst_ref=land[s].at[j],
        send_sem=send_sems.at[j * ns + s], recv_sem=recv_sems.at[j * ns + s],
        device_id=(cx, cy, c), device_id_type=MESH_ID)
        for j, (cx, cy) in enumerate(_other_chips(x, y)) for s in range(ns)]


def _chip_exchange(sections, parts, name):
    ns = len(sections)

    def body(*refs):
        copies = _chip_copies(sections, refs[:ns], refs[ns:2 * ns], *refs[2 * ns:])
        for cp in copies:
            cp.start()
        for cp in copies:
            cp.wait_recv()
        for cp in copies:
            cp.wait_send()

    n = 3 * ns
    return pl.pallas_call(
        body, name=name,
        out_shape=tuple(jax.ShapeDtypeStruct((3, rows, cols), BF16) for _, rows, cols in sections),
        in_specs=[ANY] * ns, out_specs=(ANY,) * ns,
        scratch_shapes=[pltpu.SemaphoreType.DMA((n,)), pltpu.SemaphoreType.DMA((n,))],
    )(*parts)


def _chip_exchange_start(sections, parts, name):
    ns = len(sections)

    def body(*refs):
        p_refs, land = refs[:ns], refs[ns:2 * ns]
        send_sems, recv_sems = refs[2 * ns], refs[2 * ns + 1]
        token = refs[-1]
        for cp in _chip_copies(sections, p_refs, land, send_sems, recv_sems):
            cp.start()
        token[...] = jnp.zeros_like(token)

    zones = [lax.empty((3, rows, cols), BF16) for _, rows, cols in sections]
    out = pl.pallas_call(
        body, name=name,
        out_shape=(pltpu.SemaphoreType.DMA((3 * ns,)), pltpu.SemaphoreType.DMA((3 * ns,)),
                   *[pltpu.HBM(a.shape, a.dtype) for a in parts], *[pltpu.HBM(a.shape, a.dtype) for a in zones],
                   jax.ShapeDtypeStruct((8, LANES), F32)),
        in_specs=[HBM] * (2 * ns),
        out_specs=(SEM, SEM, *[HBM] * (2 * ns), pl.BlockSpec(memory_space=pltpu.VMEM)),
        input_output_aliases={i: 2 + i for i in range(2 * ns)},
        compiler_params=pltpu.CompilerParams(has_side_effects=EFFECT),
    )(*[_in_hbm(a) for a in parts], *[_in_hbm(a) for a in zones])
    return out[0], out[1], out[2:2 + ns], out[2 + ns:2 + 2 * ns], out[-1]


def _chip_exchange_wait(sections, send_sems, recv_sems, parts, zones, after, name):
    ns = len(sections)

    def body(*refs):
        p_refs, land = refs[:ns], refs[ns:2 * ns]
        for cp in _chip_copies(sections, p_refs, land, refs[2 * ns], refs[2 * ns + 1]):
            cp.wait_send()
            cp.wait_recv()

    out = pl.pallas_call(
        body, name=name,
        out_shape=tuple(pltpu.HBM(a.shape, a.dtype) for a in (*parts, *zones)),
        in_specs=[HBM] * (2 * ns) + [SEM, SEM, ANY],
        out_specs=(HBM,) * (2 * ns),
        input_output_aliases={i: i for i in range(2 * ns)},
        compiler_params=pltpu.CompilerParams(has_side_effects=EFFECT),
    )(*parts, *zones, send_sems, recv_sems, after)
    return out[:ns], out[ns:]


def _grad_finish(sections, parts, far, chip, name):
    ns = len(sections)

    def body(chip_ref, *refs):
        p_refs, b_refs, g_refs = refs[:ns], refs[ns:2 * ns], refs[2 * ns:]
        for s in range(ns):
            g = p_refs[s][0].astype(F32)
            for j in range(3):
                g = g + b_refs[s][j].astype(F32)
            g_refs[s][...] = g

    half = [(rows // 2, cols) for _, rows, cols in sections]
    return pl.pallas_call(
        body, name=name,
        out_shape=tuple(jax.ShapeDtypeStruct((rows, cols), F32) for _, rows, cols in sections),
        grid_spec=pltpu.PrefetchScalarGridSpec(
            num_scalar_prefetch=1, grid=(2,),
            in_specs=[pl.BlockSpec((1, r, c), lambda i, chip: (chip[0], i, 0)) for r, c in half]
            + [pl.BlockSpec((3, r, c), lambda i, chip: (0, i, 0)) for r, c in half],
            out_specs=tuple(pl.BlockSpec((r, c), lambda i, chip: (i, 0)) for r, c in half)),
        compiler_params=_params(dimension_semantics=("parallel",)),
    )(chip, *parts, *far)


def _sum_devices(parts, rows, name):
    cols = parts.shape[1]
    tr = rows // 2

    def body(*refs):
        s = refs[0][...].astype(F32)
        for d in range(1, N_DEV):
            s = s + refs[d][...].astype(F32)
        refs[N_DEV][...] = s

    return pl.pallas_call(
        body, name=name,
        out_shape=jax.ShapeDtypeStruct((rows, cols), F32),
        grid=(2,),
        in_specs=[pl.BlockSpec((tr, cols), lambda i, d=d: (2 * d + i, 0)) for d in range(N_DEV)],
        out_specs=pl.BlockSpec((tr, cols), lambda i: (i, 0)),
        compiler_params=_params(dimension_semantics=("parallel",)),
    )(*([parts] * N_DEV))


def _adamw_step(w_ref, g_ref, m_ref, v_ref, d_ref, nm_ref, nv_ref):
    c1 = 1.0 / (1.0 - ADAM_B1 ** ADAM_STEP)
    c2 = 1.0 / (1.0 - ADAM_B2 ** ADAM_STEP)
    gv = g_ref[...]
    nm = ADAM_B1 * m_ref[...] + (1.0 - ADAM_B1) * gv
    nv = ADAM_B2 * v_ref[...] + (1.0 - ADAM_B2) * (gv * gv)
    nm_ref[...] = nm
    nv_ref[...] = nv
    d_ref[...] = (-ADAM_LR) * ((nm * c1) / (jnp.sqrt(nv * c2) + ADAM_EPS) + ADAM_WD * w_ref[...])


def _adamw_small(params, name):
    n = len(params)

    def body(*refs):
        for k in range(n):
            _adamw_step(*refs[4 * k:4 * k + 4], *refs[4 * n + 3 * k:4 * n + 3 * k + 3])

    out = pl.pallas_call(
        body, name=name,
        out_shape=tuple(jax.ShapeDtypeStruct(p[0].shape, F32) for p in params for _ in range(3)),
    )(*[a for p in params for a in p])
    return [out[3 * k:3 * k + 3] for k in range(n)]


def _adamw(w, g, m, v, name):
    rows, cols = w.shape
    tr = rows
    while tr * cols * 4 > (1 << 20) and tr % 16 == 0:
        tr //= 2

    def body(*refs):
        _adamw_step(*refs)

    spec = pl.BlockSpec((tr, cols), lambda i: (i, 0))
    shape = jax.ShapeDtypeStruct((rows, cols), F32)
    return pl.pallas_call(
        body, name=name,
        out_shape=(shape, shape, shape),
        grid=(rows // tr,),
        in_specs=[spec] * 4, out_specs=(spec,) * 3,
        compiler_params=_params(dimension_semantics=("parallel",)),
    )(w, g, m, v)


NAMES = ("ln1_g", "w_in", "b_in", "rpb", "w_att_o", "conv_w", "conv_b", "w_rg_a", "b_rg_a", "w_rg_i",
         "b_rg_i", "lru_lambda", "w_rec_o", "w_out", "ln2_g", "w_ff1", "w_ff2", "lnf_g")
TRANSPOSED = {"w_in": "w_in_t", "w_att_o": "w_att_o_t", "w_ff1": "w_ff1_t"}
ROW_SHARDED = ("w_rec_o", "w_out", "w_ff2")
REPLICATED = (("ln1_g", (1, D)), ("b_in", (1, D_IN)), ("rpb", (N_HEADS * N_RPB_R, N_RPB_C)),
              ("conv_b", (1, D_REC)), ("w_rg_a", (2 * N_REC_BLOCKS * REC_BLOCK, REC_BLOCK)),
              ("w_rg_i", (2 * N_REC_BLOCKS * REC_BLOCK, REC_BLOCK)), ("ln2_g", (1, D)), ("lnf_g", (1, D)))
GATE_BLOCKS = ("w_rg_a", "w_rg_i")
SMALL_ROWS = 112


def _chan_bits(vectors):
    chan = jnp.concatenate(vectors, axis=0)
    bits = lax.bitcast_convert_type(chan, BF16).reshape(-1)
    return jnp.pad(bits, (0, CHAN_BLOCK_ROWS * D - bits.shape[0])).reshape(CHAN_BLOCK_ROWS, D)


def _chan_from_bits(gathered):
    bits = gathered.reshape(N_DEV, CHAN_BLOCK_ROWS * D)[:, :2 * N_CHAN_ROWS * LANES]
    chan = lax.bitcast_convert_type(bits.reshape(N_DEV, N_CHAN_ROWS, LANES, 2), F32)
    return chan.transpose(1, 0, 2).reshape(N_CHAN_ROWS, D)


def kernel(x, ln1_g, w_in, b_in, rpb, w_att_o, conv_w, conv_b, w_rg_a, b_rg_a, w_rg_i, b_rg_i, lru_lambda, w_rec_o, w_out, ln2_g, w_ff1, w_ff2, lnf_g, loss_target, m_ln1_g, m_w_in, m_b_in, m_rpb, m_w_att_o, m_conv_w, m_conv_b, m_w_rg_a, m_b_rg_a, m_w_rg_i, m_b_rg_i, m_lru_lambda, m_w_rec_o, m_w_out, m_ln2_g, m_w_ff1, m_w_ff2, m_lnf_g, v_ln1_g, v_w_in, v_b_in, v_rpb, v_w_att_o, v_conv_w, v_conv_b, v_w_rg_a, v_b_rg_a, v_w_rg_i, v_b_rg_i, v_lru_lambda, v_w_rec_o, v_w_out, v_ln2_g, v_w_ff1, v_w_ff2, v_lnf_g):
    w = dict(zip(NAMES, (ln1_g, w_in, b_in, rpb, w_att_o, conv_w, conv_b, w_rg_a, b_rg_a, w_rg_i,
                         b_rg_i, lru_lambda, w_rec_o, w_out, ln2_g, w_ff1, w_ff2, lnf_g)))
    m = dict(zip(NAMES, (m_ln1_g, m_w_in, m_b_in, m_rpb, m_w_att_o, m_conv_w, m_conv_b, m_w_rg_a,
                         m_b_rg_a, m_w_rg_i, m_b_rg_i, m_lru_lambda, m_w_rec_o, m_w_out, m_ln2_g,
                         m_w_ff1, m_w_ff2, m_lnf_g)))
    v = dict(zip(NAMES, (v_ln1_g, v_w_in, v_b_in, v_rpb, v_w_att_o, v_conv_w, v_conv_b, v_w_rg_a,
                         v_b_rg_a, v_w_rg_i, v_b_rg_i, v_lru_lambda, v_w_rec_o, v_w_out, v_ln2_g,
                         v_w_ff1, v_w_ff2, v_lnf_g)))
    xi, yi, ci = _position()

    shard = {t: w[n][0].T.astype(BF16) for n, t in TRANSPOSED.items()}
    shard.update({n: w[n][0].astype(BF16) for n in ROW_SHARDED})
    shard["chan"] = _chan_bits([w[n][0] for n, _ in CHAN])
    first, later = ("w_in_t", "chan"), ("w_rec_o", "w_out", "w_att_o_t", "w_ff1_t", "w_ff2")
    *gathered, done = _all_gather([shard[n] for n in first], "weight_all_gather")
    p = dict(zip(first, gathered))
    send_sems, recv_sems, sent, zones, token = _gather_start([shard[n] for n in later], done,
                                                             "weight_gather_start")

    def late_weights(after):
        landed = _gather_wait(send_sems, recv_sems, sent, zones, after, "weight_gather_wait")
        return dict(zip(later, _gather_pass_on([shard[n].shape[0] for n in later], landed,
                                               "weight_gather_pass_on")))

    chan = _chan_from_bits(p.pop("chan"))
    r0 = 0
    for n, rows in CHAN:
        p[n] = chan[r0:r0 + rows]
        r0 += rows
    p.update(ln1_g=w["ln1_g"], b_in=w["b_in"] + token[0, 0], rpb=w["rpb"][0], conv_b=w["conv_b"],
             w_rg_a=w["w_rg_a"][0], w_rg_i=w["w_rg_i"][0], ln2_g=w["ln2_g"],
             lnf_g=w["lnf_g"].reshape(1, D))

    core = jnp.reshape(ci, (1,)).astype(jnp.int32)
    chip = jnp.reshape(2 * xi + yi, (1,)).astype(jnp.int32)
    early_sections, late_sections = SECTIONS[1:], SECTIONS[:1]
    in_flight = {}

    def pair_sum_and_send(group, sections, after):
        send_sems, recv_sems, sect, zones, _ = in_flight["pair_" + group]
        sect, got = _pair_exchange_wait(sections, send_sems, recv_sems, sect, zones, after,
                                        "grad_pair_exchange_wait_" + group)
        parts = _pair_add(sections, sect, got, core, "grad_pair_add_" + group)
        in_flight[group] = _chip_exchange_start(sections, parts, "grad_chip_exchange_start_" + group)
        return in_flight[group][-1]

    def reduce_early(grads):
        chan_g = jnp.concatenate([grads[n] for n, _ in CHAN], axis=0)
        chan_g = chan_g.reshape(N_CHAN_ROWS, N_DEV, LANES).transpose(1, 0, 2).astype(BF16)
        chan_g = jnp.pad(chan_g.reshape(N_DEV, -1), ((0, 0), (0, CHAN_BLOCK_ROWS * D - N_CHAN_ROWS * LANES)))
        grads["chan"] = chan_g.reshape(N_DEV * CHAN_BLOCK_ROWS, D)
        grads["gates"] = jnp.concatenate([grads[n].reshape(-1, D) for n in GATE_BLOCKS], axis=0).astype(BF16)
        in_flight["pair_early"] = _pair_exchange_start(
            early_sections, [grads[n] for n, _, _ in early_sections], PAIR_EARLY_ID,
            "grad_pair_exchange_start_early")
        return pair_sum_and_send("early", early_sections, in_flight["pair_early"][-1])[0, 0]

    loss_part, grad_x, grads = _local_step(x[0], loss_target[0], p, late_weights, reduce_early)
    in_flight["pair_late"] = _pair_exchange_start(
        late_sections, [grads[n] for n, _, _ in late_sections], PAIR_LATE_ID, "grad_pair_exchange_start_late")

    def finish(group, sections, after, name):
        send_sems, recv_sems, parts, zones, _ = in_flight[group]
        parts, far = _chip_exchange_wait(sections, send_sems, recv_sems, parts, zones, after,
                                         "grad_chip_exchange_wait_" + name)
        return dict(zip((n for n, _, _ in sections),
                        _grad_finish(sections, parts, far, chip, "grad_finish_" + name)))

    summed = finish("early", early_sections, in_flight["pair_late"][-1], "early")
    started_late = pair_sum_and_send("late", late_sections, summed["gates"])

    flat = jnp.concatenate([grads[n].reshape(-1) for n, _ in REPLICATED if n not in GATE_BLOCKS]
                           + [loss_part.reshape(-1) + started_late[0, 0]])
    n_small = flat.shape[0]
    flat = jnp.pad(flat, (0, SMALL_ROWS * LANES - n_small)).reshape(SMALL_ROWS, LANES)
    small_parts, gate_sum, _ = _all_gather([flat, summed["gates"]], "small_grad_all_gather")
    small = _sum_devices(small_parts, SMALL_ROWS, "small_grad_sum").reshape(-1)
    loss = small[n_small - 1]

    g, delta, new_m, new_v = {}, {}, {}, {}

    def update(n, g2, shape2):
        d2, m2, v2 = _adamw(w[n].reshape(shape2), g2, m[n].reshape(shape2), v[n].reshape(shape2),
                            "adamw_" + n)
        g[n], delta[n], new_m[n], new_v[n] = (a.reshape(w[n].shape) for a in (g2, d2, m2, v2))

    small_params = []
    o = 0
    for n, shape2 in REPLICATED:
        if n in GATE_BLOCKS:
            k, rows = GATE_BLOCKS.index(n), gate_sum.shape[0] // len(GATE_BLOCKS)
            update(n, gate_sum[k * rows:(k + 1) * rows].reshape(shape2), shape2)
        else:
            size = shape2[0] * shape2[1]
            small_params.append((n, small[o:o + size].reshape(shape2), shape2))
            o += size
    chan_back = summed["chan"].reshape(-1)[:N_CHAN_ROWS * LANES].reshape(N_CHAN_ROWS, LANES)
    r0 = 0
    for n, rows in CHAN:
        small_params.append((n, chan_back[r0:r0 + rows], (rows, LANES)))
        r0 += rows
    results = _adamw_small([(w[n].reshape(s2), g2, m[n].reshape(s2), v[n].reshape(s2))
                            for n, g2, s2 in small_params], "adamw_vectors")
    for (n, g2, _), (d2, m2, v2) in zip(small_params, results):
        g[n], delta[n], new_m[n], new_v[n] = (a.reshape(w[n].shape) for a in (g2, d2, m2, v2))

    for n in ROW_SHARDED:
        update(n, summed[n], summed[n].shape)
    for n, t in TRANSPOSED.items():
        if t in summed:
            update(n, summed[t].T, summed[t].shape[::-1])
    summed = finish("late", late_sections, _after_all(list(delta.values()), "updates_done"), "late")
    update("w_in", summed["w_in_t"].T, summed["w_in_t"].shape[::-1])

    return (loss, grad_x[None], *[g[n] for n in NAMES], *[delta[n] for n in NAMES],
            *[new_m[n] for n in NAMES], *[new_v[n] for n in NAMES])
```

```python
import math

import numpy as np
import jax
import jax.numpy as jnp
from jax import lax
from jax.experimental import pallas as pl
from jax.experimental.pallas import tpu as pltpu

F32 = jnp.float32
BF16 = jnp.bfloat16

T = 2048
D = 1024
D_ATT = 512
D_REC = 1024
D_FF = 4096
D_IN = 5632
N_HEADS = 8
DH = 64
GRID_W = 64
ROWS = T // GRID_W
WIN_H = 8
WIN_W = 16
KWIN = WIN_H * GRID_W
N_RPB_R = 2 * WIN_H - 1
N_RPB_C = 2 * WIN_W - 1
N_REC_BLOCKS = 16
REC_BLOCK = 64
CG = 128
N_CG = D_REC // CG
LRU_C = 8.0
EPS = 1e-6
N_DEV = 8
N_CHIPS = 4
LANES = 128

ADAM_LR = 0.001
ADAM_B1 = 0.9
ADAM_B2 = 0.999
ADAM_EPS = 1e-08
ADAM_WD = 0.01
ADAM_STEP = 10

MESH_AXES = ("x", "y", "c")
VMEM_LIMIT = 56 * 1024 * 1024

TILE = 512
DZ_ARRAYS = ((0, 3, 1), (3, 4, 2), (7, 4, 2))
N_DZ_TILES = D_IN // TILE


def _params(**kw):
    return pltpu.CompilerParams(vmem_limit_bytes=VMEM_LIMIT, **kw)


HG = 4
HQ = HG * GRID_W
HC = HG * DH


def _att_tables():
    rq = np.arange(GRID_W)
    kc = np.arange(KWIN) % GRID_W
    win_start = np.clip(rq - WIN_W // 2, 0, GRID_W - WIN_W)
    valid = (kc[None, :] >= win_start[:, None]) & (kc[None, :] < win_start[:, None] + WIN_W)
    same_head = (np.arange(HQ)[:, None] // GRID_W) == (np.arange(HC)[None, :] // DH)
    return valid.astype(np.float32), same_head.astype(np.float32)


def _pair_mask():
    half = np.arange(2 * DH) // DH
    return (half[:, None] == half[None, :]).astype(np.float32)


def _dup_table():
    return np.concatenate([np.eye(REC_BLOCK, dtype=np.float32)] * 2, axis=1)


def _sigmoid(x):
    return 0.5 * jnp.tanh(0.5 * x) + 0.5


def _softplus(x):
    return jnp.maximum(x, 0.0) + jnp.log(1.0 + jnp.exp(-jnp.abs(x)))


def _one_minus_square(log_a, a):
    x = 2.0 * log_a
    series = -x * (1.0 + x * (0.5 + x * (1.0 / 6.0)))
    return jnp.where(x > -0.02, series, 1.0 - a * a)


_GELU_C = math.sqrt(2.0 / math.pi)


def _gelu_and_grad(x):
    x2 = x * x
    inner = _GELU_C * (x + 0.044715 * x * x2)
    t = jnp.tanh(inner)
    g = 0.5 * x * (1.0 + t)
    dg = 0.5 * (1.0 + t) + 0.5 * x * (1.0 - t * t) * _GELU_C * (1.0 + 3.0 * 0.044715 * x2)
    return g, dg


def _dot(a, b):
    return jnp.dot(a, b, preferred_element_type=F32)


def _dot_nt(a, b):
    return lax.dot_general(a, b, (((1,), (1,)), ((), ())), preferred_element_type=F32)


def _dot_tn(a, b):
    return lax.dot_general(a, b, (((0,), (0,)), ((), ())), preferred_element_type=F32)


def _dot_exact(a, b):
    return jnp.dot(a, b, precision=lax.Precision.HIGHEST, preferred_element_type=F32)


def _shift_rows(x, s):
    n = x.shape[0]
    rows = lax.broadcasted_iota(jnp.int32, x.shape, 0)
    y = pltpu.roll(x, s % n, 0)
    if s > 0:
        return jnp.where(rows >= s, y, 0.0)
    return jnp.where(rows < n + s, y, 0.0)


def _rms_bwd(dh, xh, r, g):
    dxh = dh * g
    return r * (dxh - xh * jnp.mean(dxh * xh, axis=-1, keepdims=True))


def _matmul(a, b, mode, out_dtype, name, tm=512, tn=1024, tk=2048):
    if mode == "nn":
        (m, k), (k2, n) = a.shape, b.shape
    elif mode == "nt":
        (m, k), (n, k2) = a.shape, b.shape
    else:
        (k, m), (k2, n) = a.shape, b.shape
    assert k == k2
    tm, tn, tk = min(tm, m), min(tn, n), min(tk, k)
    assert m % tm == 0 and n % tn == 0 and k % tk == 0
    nk = k // tk
    dot = {"nn": _dot, "nt": _dot_nt, "tn": _dot_tn}[mode]

    def body(a_ref, b_ref, o_ref, acc):
        kk = pl.program_id(2)
        part = dot(a_ref[...].astype(BF16), b_ref[...].astype(BF16))
        if nk == 1:
            o_ref[...] = part.astype(out_dtype)
            return

        @pl.when(kk == 0)
        def _():
            acc[...] = part

        @pl.when(kk > 0)
        def _():
            acc[...] += part

        @pl.when(kk == nk - 1)
        def _():
            o_ref[...] = acc[...].astype(out_dtype)

    if mode == "tn":
        a_spec = pl.BlockSpec((tk, tm), lambda i, j, kk: (kk, i))
    else:
        a_spec = pl.BlockSpec((tm, tk), lambda i, j, kk: (i, kk))
    if mode == "nt":
        b_spec = pl.BlockSpec((tn, tk), lambda i, j, kk: (j, kk))
    else:
        b_spec = pl.BlockSpec((tk, tn), lambda i, j, kk: (kk, j))
    return pl.pallas_call(
        body, name=name,
        out_shape=jax.ShapeDtypeStruct((m, n), out_dtype),
        grid=(m // tm, n // tn, nk),
        in_specs=[a_spec, b_spec],
        out_specs=pl.BlockSpec((tm, tn), lambda i, j, kk: (i, j)),
        scratch_shapes=[pltpu.VMEM((tm, tn) if nk > 1 else (8, LANES), F32)],
        compiler_params=_params(dimension_semantics=("parallel", "parallel", "arbitrary")),
    )(a, b)


def _in_proj(x, g1, w_in_t, b_in):
    tm = 512

    def body(x_ref, g_ref, w_hbm, b_ref, qkv_ref, uy_ref, gg_ref, h_ref, w):
        @pl.when(pl.program_id(0) == 0)
        def _():
            pltpu.sync_copy(w_hbm, w)

        xv = x_ref[...]
        r = lax.rsqrt(jnp.mean(xv * xv, axis=-1, keepdims=True) + EPS)
        h = ((xv * r) * g_ref[...]).astype(BF16)
        h_ref[...] = h
        row0 = 0
        for ref in (qkv_ref, uy_ref, gg_ref):
            for c0 in range(0, ref.shape[1], TILE):
                z = _dot_nt(h, w[row0:row0 + TILE, :]) + b_ref[:, row0:row0 + TILE]
                ref[:, c0:c0 + TILE] = z.astype(ref.dtype)
                row0 += TILE

    tok = lambda width: pl.BlockSpec((tm, width), lambda i: (i, 0))
    return pl.pallas_call(
        body, name="in_proj",
        out_shape=(jax.ShapeDtypeStruct((T, 3 * D_ATT), BF16),
                   jax.ShapeDtypeStruct((T, 2 * D_REC), F32),
                   jax.ShapeDtypeStruct((T, 2 * D), F32),
                   jax.ShapeDtypeStruct((T, D), BF16)),
        grid=(T // tm,),
        in_specs=[tok(D), pl.BlockSpec((1, D), lambda i: (0, 0)), pl.BlockSpec(memory_space=pl.ANY),
                  pl.BlockSpec((1, D_IN), lambda i: (0, 0))],
        out_specs=(tok(3 * D_ATT), tok(2 * D_REC), tok(2 * D), tok(D)),
        scratch_shapes=[pltpu.VMEM((D_IN, D), BF16)],
        compiler_params=_params(dimension_semantics=("arbitrary",)),
    )(x, g1, w_in_t, b_in)


def _dz_specs(rows, tile_of, row_of):
    def spec(off, n, per_plane):
        def index(*ids):
            t = jnp.clip(tile_of(*ids) - off, 0, n - 1)
            return (t // per_plane, row_of(*ids), t % per_plane)
        return pl.BlockSpec((1, rows, TILE), index)
    return [spec(off, n, per) for off, n, per in DZ_ARRAYS]


def _dh_norm1_bwd(dz, w_in_t, x, g1, dx1):
    tm = 512

    def body(dqkv_ref, duy_ref, dgg_ref, w_hbm, x_ref, g_ref, dx1_ref, gx_ref, dg_ref, w):
        @pl.when(pl.program_id(0) == 0)
        def _():
            pltpu.sync_copy(w_hbm, w)
            dg_ref[...] = jnp.zeros_like(dg_ref)

        dh, row0 = None, 0
        for ref in (dqkv_ref, duy_ref, dgg_ref):
            for plane in range(ref.shape[0]):
                cols = ref.shape[2]
                part = _dot(ref[plane], w[row0:row0 + cols, :])
                dh = part if dh is None else dh + part
                row0 += cols
        xv = x_ref[...]
        r = lax.rsqrt(jnp.mean(xv * xv, axis=-1, keepdims=True) + EPS)
        xh = xv * r
        dg_ref[...] += jnp.sum(dh * xh, axis=0, keepdims=True)
        gx_ref[...] = dx1_ref[...] + _rms_bwd(dh, xh, r, g_ref[...])

    tok = pl.BlockSpec((tm, D), lambda i: (i, 0))
    vec = pl.BlockSpec((1, D), lambda i: (0, 0))
    planes = lambda a: pl.BlockSpec((a.shape[0], tm, a.shape[2]), lambda i: (0, i, 0))
    return pl.pallas_call(
        body, name="dh_norm1_bwd",
        out_shape=(jax.ShapeDtypeStruct((T, D), F32), jax.ShapeDtypeStruct((1, D), F32)),
        grid=(T // tm,),
        in_specs=[planes(a) for a in dz] + [pl.BlockSpec(memory_space=pl.ANY), tok, vec, tok],
        out_specs=(tok, vec),
        scratch_shapes=[pltpu.VMEM((D_IN, D), BF16)],
        compiler_params=_params(dimension_semantics=("arbitrary",)),
    )(*dz, w_in_t, x, g1, dx1)


def _grad_w_in(dz, h):
    def body(*refs):
        seg_refs = refs[:3]
        h_ref, gw_ref, gb_ref = refs[3:]
        j = pl.program_id(0)

        for s, (off, n, _) in enumerate(DZ_ARRAYS):
            @pl.when((j >= off) & (j < off + n))
            def _(s=s):
                a = seg_refs[s][0]
                gw_ref[...] = _dot_tn(a, h_ref[...]).astype(BF16)
                gb_ref[...] = jnp.sum(a.astype(F32), axis=0, keepdims=True)

    return pl.pallas_call(
        body, name="grad_w_in",
        out_shape=(jax.ShapeDtypeStruct((D_IN, D), BF16), jax.ShapeDtypeStruct((1, D_IN), F32)),
        grid=(N_DZ_TILES,),
        in_specs=_dz_specs(T, lambda j: j, lambda j: 0) + [pl.BlockSpec((T, D), lambda j: (0, 0))],
        out_specs=(pl.BlockSpec((TILE, D), lambda j: (j, 0)), pl.BlockSpec((1, TILE), lambda j: (0, j))),
        compiler_params=_params(dimension_semantics=("parallel",)),
    )(*dz, h)


def _rpb_rows(rpb):
    padded = jnp.pad(rpb, ((0, 0), (0, 0), (0, GRID_W - N_RPB_C)))
    rows = [padded[:, WIN_H - 1 - oi: 2 * WIN_H - 1 - oi].reshape(N_HEADS // HG, HG, KWIN)
            for oi in range(WIN_H)]
    return jnp.stack(rows, axis=0)


SKEW = KWIN - (WIN_W - 1)


MASKED = -1e30


def _bias_tiles(rows_ref, valid, bias_s):
    for oi in range(WIN_H):
        for hh in range(HG):
            row = jnp.broadcast_to(rows_ref[oi, 0, hh:hh + 1, :], (GRID_W, KWIN))
            tile = pltpu.roll(row, SKEW, 1, stride=1, stride_axis=0)
            bias_s[oi, hh * GRID_W:(hh + 1) * GRID_W, :] = jnp.where(valid, tile, MASKED)


def _bias_tile_grads(gb_s, flip, out_ref):
    for oi in range(WIN_H):
        for hh in range(HG):
            g = _dot_exact(flip, gb_s[oi, hh * GRID_W:(hh + 1) * GRID_W, :])
            back = pltpu.roll(g, KWIN - (GRID_W - WIN_W), 1, stride=1, stride_axis=0)
            out_ref[0, oi, hh:hh + 1, :] = jnp.sum(back, axis=0, keepdims=True)


def _rpb_fold(row_grads):
    g = row_grads.transpose(1, 0, 2, 3).reshape(WIN_H, N_HEADS, WIN_H, GRID_W)
    g = g.transpose(0, 2, 1, 3)

    def body(g_ref, o_ref):
        for dr in range(N_RPB_R):
            terms = [g_ref[oi, i] for oi in range(WIN_H) for i in range(WIN_H) if i - oi + WIN_H - 1 == dr]
            acc = terms[0]
            for term in terms[1:]:
                acc = acc + term
            o_ref[dr] = acc

    out = pl.pallas_call(
        body, name="rpb_fold",
        out_shape=jax.ShapeDtypeStruct((N_RPB_R, N_HEADS, GRID_W), F32),
    )(g)
    return out.transpose(1, 0, 2)[:, :, :N_RPB_C]


ATT_GROUPS = N_HEADS // HG
ATT_UNROLL = 2


def _stacked(rows64, same_head):
    return jnp.where(same_head, jnp.concatenate([rows64] * HG, axis=0), jnp.zeros((), BF16))


def _own_heads(stacked):
    head = lax.broadcasted_iota(jnp.int32, (GRID_W, HC), 1) // DH
    out = stacked[:GRID_W]
    for h in range(1, HG):
        out = jnp.where(head == h, stacked[h * GRID_W:(h + 1) * GRID_W], out)
    return out


def _att_scores(q_ref, k_ref, bias_ref, same_head, r):
    rs = jnp.clip(r - WIN_H // 2, 0, ROWS - WIN_H)
    oi = r - rs
    q0 = pl.multiple_of(r * GRID_W, GRID_W)
    k0 = pl.multiple_of(rs * GRID_W, GRID_W)
    q2 = _stacked(q_ref[pl.ds(q0, GRID_W), :] * (DH ** -0.5), same_head)
    kw = k_ref[pl.ds(k0, KWIN), :]
    s = _dot_nt(q2, kw) + bias_ref[oi]
    e = jnp.exp(s - jnp.max(s, axis=-1, keepdims=True))
    return e, 1.0 / jnp.sum(e, axis=-1, keepdims=True), q2, kw, q0, k0, oi


def _att_specs():
    col = lambda off: pl.BlockSpec((T, HC), lambda g: (0, g + off * ATT_GROUPS))
    tables = [pl.BlockSpec((WIN_H, 1, HG, KWIN), lambda g: (0, g, 0, 0)),
              pl.BlockSpec((GRID_W, KWIN), lambda g: (0, 0)),
              pl.BlockSpec((HQ, HC), lambda g: (0, 0))]
    return col, tables, pltpu.VMEM((WIN_H, HQ, KWIN), F32)


def _att_fwd(qkv, bias_rows):
    valid_np, same_head_np = _att_tables()

    def body(q_ref, k_ref, v_ref, rows_ref, valid_ref, head_ref, o_ref, bias_s):
        same_head = head_ref[...] > 0.5
        _bias_tiles(rows_ref, valid_ref[...] > 0.5, bias_s)

        def row(r, carry):
            e, rl, _, _, q0, k0, _ = _att_scores(q_ref, k_ref, bias_s, same_head, r)
            o2 = _dot((e * rl).astype(BF16), v_ref[pl.ds(k0, KWIN), :])
            o_ref[pl.ds(q0, GRID_W), :] = _own_heads(o2).astype(BF16)
            return carry

        lax.fori_loop(0, ROWS, row, 0, unroll=ATT_UNROLL)

    col, tables, tiles = _att_specs()
    return pl.pallas_call(
        body, name="att_fwd",
        out_shape=jax.ShapeDtypeStruct((T, D_ATT), BF16),
        grid=(ATT_GROUPS,),
        in_specs=[col(0), col(1), col(2)] + tables,
        out_specs=col(0),
        scratch_shapes=[tiles],
        compiler_params=_params(dimension_semantics=("parallel",)),
    )(qkv, qkv, qkv, bias_rows, jnp.asarray(valid_np), jnp.asarray(same_head_np))


def _att_bwd(qkv, bias_rows, datt, after):
    valid_np, same_head_np = _att_tables()

    def body(q_ref, k_ref, v_ref, do_ref, rows_ref, valid_ref, head_ref, flip_ref,
             dqkv_ref, grows_ref, dk_acc, dv_acc, bias_s, gb_s):
        same_head = head_ref[...] > 0.5
        dk_acc[...] = jnp.zeros_like(dk_acc)
        dv_acc[...] = jnp.zeros_like(dv_acc)
        gb_s[...] = jnp.zeros_like(gb_s)
        _bias_tiles(rows_ref, valid_ref[...] > 0.5, bias_s)

        def row(r, carry):
            e, rl, q2, kw, q0, k0, oi = _att_scores(q_ref, k_ref, bias_s, same_head, r)
            do2 = _stacked(do_ref[pl.ds(q0, GRID_W), :], same_head)
            vw = v_ref[pl.ds(k0, KWIN), :]
            p = e * rl
            dp = _dot_nt(do2, vw)
            ds = p * (dp - jnp.sum(dp * p, axis=-1, keepdims=True))
            p16 = p.astype(BF16)
            ds16 = ds.astype(BF16)
            dv_acc[pl.ds(k0, KWIN), :] += _dot_tn(p16, do2)
            dk_acc[pl.ds(k0, KWIN), :] += _dot_tn(ds16, q2)
            dq2 = _dot(ds16, kw) * (DH ** -0.5)
            dqkv_ref[0, pl.ds(q0, GRID_W), :] = _own_heads(dq2).astype(BF16)
            gb_s[oi] += ds
            return carry

        lax.fori_loop(0, ROWS, row, 0, unroll=ATT_UNROLL)
        dqkv_ref[1] = dk_acc[...].astype(BF16)
        dqkv_ref[2] = dv_acc[...].astype(BF16)
        _bias_tile_grads(gb_s, flip_ref[...], grows_ref)

    col, tables, tiles = _att_specs()
    return pl.pallas_call(
        body, name="att_bwd",
        out_shape=(jax.ShapeDtypeStruct((3, T, D_ATT), BF16),
                   jax.ShapeDtypeStruct((ATT_GROUPS, WIN_H, HG, KWIN), F32)),
        grid=(ATT_GROUPS,),
        in_specs=[col(0), col(1), col(2), col(0)] + tables + [pl.BlockSpec((GRID_W, GRID_W), lambda g: (0, 0))],
        out_specs=(pl.BlockSpec((3, T, HC), lambda g: (0, 0, g)),
                   pl.BlockSpec((1, WIN_H, HG, KWIN), lambda g: (g, 0, 0, 0))),
        scratch_shapes=[pltpu.VMEM((T, HC), F32), pltpu.VMEM((T, HC), F32), tiles, tiles],
        compiler_params=_params(dimension_semantics=("parallel",)),
    )(qkv, qkv, qkv, datt, bias_rows, jnp.asarray(valid_np) + after, jnp.asarray(same_head_np),
      jnp.asarray(np.eye(GRID_W, dtype=np.float32)[::-1].copy()))


def _conv_taps(up):
    return (_shift_rows(up, 2), _shift_rows(up, 1), up, _shift_rows(up, -1))


def _pair_block_diag(w_pair, dup, same_half):
    return jnp.where(same_half, _dot(w_pair.astype(BF16), dup), 0.0).astype(BF16)


def _gates(u, u16, wa, ba, wi, bi, lam):
    r = _sigmoid(_dot(u16, wa) + ba)
    ig = _sigmoid(_dot(u16, wi) + bi)
    sp = _softplus(-lam)
    log_a = (-LRU_C) * r * sp
    a = jnp.exp(log_a)
    mult2 = jnp.maximum(_one_minus_square(log_a, a), 0.0)
    return r, ig, sp, a, jnp.sqrt(mult2), mult2


SCAN_BLOCKS = 2


def _scans(jobs):
    c = jobs[0][0].shape[1]
    nblk = T // 8
    rows = lax.broadcasted_iota(jnp.int32, (8, c), 0)

    def block(a, b, reverse):
        for s in (1, 2, 4):
            if reverse:
                keep = rows < 8 - s
                a_s = jnp.where(keep, pltpu.roll(a, 8 - s, 0), 1.0)
                b_s = jnp.where(keep, pltpu.roll(b, 8 - s, 0), 0.0)
            else:
                keep = rows >= s
                a_s = jnp.where(keep, pltpu.roll(a, s, 0), 1.0)
                b_s = jnp.where(keep, pltpu.roll(b, s, 0), 0.0)
            b = a * b_s + b
            a = a * a_s
        return a, b

    def step(i, carry):
        out = []
        for (a_ref, b_ref, h_ref, reverse), h_prev in zip(jobs, carry):
            for u in range(SCAN_BLOCKS):
                blk = i * SCAN_BLOCKS + u
                if reverse:
                    blk = nblk - 1 - blk
                t0 = pl.multiple_of(blk * 8, 8)
                a, b = block(a_ref[pl.ds(t0, 8), :], b_ref[pl.ds(t0, 8), :], reverse)
                h = a * h_prev + b
                h_ref[pl.ds(t0, 8), :] = h
                h_prev = jnp.broadcast_to(h[0:1] if reverse else h[7:8], (8, c))
            out.append(h_prev)
        return tuple(out)

    lax.fori_loop(0, nblk // SCAN_BLOCKS, step, tuple(jnp.zeros((8, c), F32) for _ in jobs))


def _rec_specs():
    tok = lambda off: pl.BlockSpec((T, CG), lambda g: (0, g + off))
    per_ch = lambda rows: pl.BlockSpec((rows, CG), lambda g: (0, g))
    wspec = pl.BlockSpec((2, 1, CG, REC_BLOCK), lambda g: (0, g, 0, 0))
    const = lambda shape: pl.BlockSpec(shape, lambda g: (0, 0))
    return tok, per_ch, wspec, const


def _rec_fwd(uy, conv_w, conv_b, w_a, b_a, w_i, b_i, lam):
    tok, per_ch, wspec, const = _rec_specs()

    def body(up_ref, yb_ref, cw_ref, cb_ref, wa_ref, ba_ref, wi_ref, bi_ref, lam_ref, dup_ref, half_ref,
             hf_ref, hb_ref, yrec_ref, am_ref, bx_f, bx_b):
        dup = dup_ref[...]
        same_half = half_ref[...] > 0.5
        taps = _conv_taps(up_ref[...])
        u = cb_ref[...]
        for j in range(4):
            u = u + taps[j] * cw_ref[j:j + 1, :]
        u16 = u.astype(BF16)
        for d, bx_s in enumerate((bx_f, bx_b)):
            wa = _pair_block_diag(wa_ref[d, 0], dup, same_half)
            wi = _pair_block_diag(wi_ref[d, 0], dup, same_half)
            _, ig, _, a, mult, _ = _gates(u, u16, wa, ba_ref[d:d + 1, :], wi, bi_ref[d:d + 1, :],
                                       lam_ref[d:d + 1, :])
            am_ref[2 * d] = a
            am_ref[2 * d + 1] = mult
            bx_s[...] = mult * (ig * u)
        _scans([(am_ref.at[0], bx_f, hf_ref, False), (am_ref.at[2], bx_b, hb_ref, True)])
        gelu, _ = _gelu_and_grad(yb_ref[...])
        yrec_ref[...] = ((hf_ref[...] + hb_ref[...]) * gelu).astype(BF16)

    return pl.pallas_call(
        body, name="rec_fwd",
        out_shape=(jax.ShapeDtypeStruct((T, D_REC), F32), jax.ShapeDtypeStruct((T, D_REC), F32),
                   jax.ShapeDtypeStruct((T, D_REC), BF16), jax.ShapeDtypeStruct((4, T, D_REC), F32)),
        grid=(N_CG,),
        in_specs=[tok(0), tok(N_CG), per_ch(4), per_ch(1), wspec, per_ch(2), wspec, per_ch(2), per_ch(2),
                  const((REC_BLOCK, CG)), const((CG, CG))],
        out_specs=(tok(0), tok(0), tok(0), pl.BlockSpec((4, T, CG), lambda g: (0, 0, g))),
        scratch_shapes=[pltpu.VMEM((T, CG), F32)] * 2,
        compiler_params=_params(dimension_semantics=("parallel",)),
    )(uy, uy, conv_w, conv_b, w_a, b_a, w_i, b_i, lam,
      jnp.asarray(_dup_table(), BF16), jnp.asarray(_pair_mask()))


def _rec_bwd(uy, hf, hb, am, dyrec, conv_w, conv_b, w_a, b_a, w_i, b_i, lam):
    tok, per_ch, wspec, const = _rec_specs()

    def body(up_ref, yb_ref, hf_ref, hb_ref, am_ref, dy_ref, cw_ref, cb_ref, wa_ref, ba_ref, wi_ref, bi_ref,
             lam_ref, dup_ref, dupt_ref, half_ref,
             duy_ref, dcw_ref, dcb_ref, dwa_ref, dba_ref, dwi_ref, dbi_ref, dlam_ref,
             a_s0, a_s1, dh_s, g_s0, g_s1):
        dup = dup_ref[...]
        dup_t = dupt_ref[...]
        same_half = half_ref[...] > 0.5
        taps = _conv_taps(up_ref[...])
        u = cb_ref[...]
        for j in range(4):
            u = u + taps[j] * cw_ref[j:j + 1, :]
        u16 = u.astype(BF16)
        gelu, dgelu = _gelu_and_grad(yb_ref[...])
        dy = dy_ref[...]
        duy_ref[1] = (dy * (hf_ref[...] + hb_ref[...]) * dgelu).astype(BF16)
        dh_s[...] = dy * gelu
        a_s0[...] = _shift_rows(am_ref[0], -1)
        a_s1[...] = _shift_rows(am_ref[2], 1)
        _scans([(a_s0, dh_s, g_s0, True), (a_s1, dh_s, g_s1, False)])
        du = jnp.zeros((T, CG), F32)
        for d, g_s in enumerate((g_s0, g_s1)):
            reverse = d == 1
            wa = _pair_block_diag(wa_ref[d, 0], dup, same_half)
            wi = _pair_block_diag(wi_ref[d, 0], dup, same_half)
            lam_d = lam_ref[d:d + 1, :]
            r = _sigmoid(_dot(u16, wa) + ba_ref[d:d + 1, :])
            ig = _sigmoid(_dot(u16, wi) + bi_ref[d:d + 1, :])
            sp = _softplus(-lam_d)
            a, mult = am_ref[2 * d], am_ref[2 * d + 1]
            mult2 = mult * mult
            g = g_s[...]
            h_prev = _shift_rows(hb_ref[...], -1) if reverse else _shift_rows(hf_ref[...], 1)
            da = g * h_prev
            dmult = g * (ig * u)
            dig = g * mult * u
            du = du + g * mult * ig
            dmult_dlog = jnp.where(mult2 > 0.0, -(a * a) * lax.rsqrt(mult2), 0.0)
            dlog_a = da * a + dmult * dmult_dlog
            dr = dlog_a * ((-LRU_C) * sp)
            dsp = jnp.sum(dlog_a * ((-LRU_C) * r), axis=0, keepdims=True)
            dlam_ref[d:d + 1, :] = dsp * (-_sigmoid(-lam_d))
            dga = dr * r * (1.0 - r)
            dgi = dig * ig * (1.0 - ig)
            dga16 = dga.astype(BF16)
            dgi16 = dgi.astype(BF16)
            du = du + _dot_nt(dga16, wa) + _dot_nt(dgi16, wi)
            dwa_ref[d, 0] = _dot_exact(jnp.where(same_half, _dot_tn(u16, dga16), 0.0), dup_t)
            dwi_ref[d, 0] = _dot_exact(jnp.where(same_half, _dot_tn(u16, dgi16), 0.0), dup_t)
            dba_ref[d:d + 1, :] = jnp.sum(dga, axis=0, keepdims=True)
            dbi_ref[d:d + 1, :] = jnp.sum(dgi, axis=0, keepdims=True)
        dcb_ref[...] = jnp.sum(du, axis=0, keepdims=True)
        for j in range(4):
            dcw_ref[j:j + 1, :] = jnp.sum(du * taps[j], axis=0, keepdims=True)
        dup_in = (_shift_rows(du, -2) * cw_ref[0:1, :] + _shift_rows(du, -1) * cw_ref[1:2, :]
                  + du * cw_ref[2:3, :] + _shift_rows(du, 1) * cw_ref[3:4, :])
        duy_ref[0] = dup_in.astype(BF16)

    wshape = jax.ShapeDtypeStruct((2, N_CG, CG, REC_BLOCK), F32)
    vec = lambda rows: jax.ShapeDtypeStruct((rows, D_REC), F32)
    dup_np = _dup_table()
    return pl.pallas_call(
        body, name="rec_bwd",
        out_shape=(jax.ShapeDtypeStruct((2, T, D_REC), BF16),
                   vec(4), vec(1), wshape, vec(2), wshape, vec(2), vec(2)),
        grid=(N_CG,),
        in_specs=[tok(0), tok(N_CG), tok(0), tok(0), pl.BlockSpec((4, T, CG), lambda g: (0, 0, g)), tok(0),
                  per_ch(4), per_ch(1), wspec, per_ch(2), wspec, per_ch(2), per_ch(2),
                  const((REC_BLOCK, CG)), const((CG, REC_BLOCK)), const((CG, CG))],
        out_specs=(pl.BlockSpec((2, T, CG), lambda g: (0, 0, g)),
                   per_ch(4), per_ch(1), wspec, per_ch(2), wspec, per_ch(2), per_ch(2)),
        scratch_shapes=[pltpu.VMEM((T, CG), F32)] * 5,
        compiler_params=_params(dimension_semantics=("parallel",)),
    )(uy, uy, hf, hb, am, dyrec, conv_w, conv_b, w_a, b_a, w_i, b_i, lam,
      jnp.asarray(dup_np, BF16), jnp.asarray(dup_np.T.copy()), jnp.asarray(_pair_mask()))


TM_MIX = 256


def _mix_specs():
    tok = lambda width, blk=0: pl.BlockSpec((TM_MIX, width), lambda i: (i, blk))
    full = lambda shape: pl.BlockSpec(shape, lambda i: (0, 0))
    return tok, full


def _mix_fwd(x, att, yrec, gg, w_att_o_t, w_rec_o, w_out):
    tok, full = _mix_specs()

    def body(x_ref, att_ref, yr_ref, ga_ref, gr_ref, wao_ref, wro_ref, wo_ref, x1_ref, mixed_ref):
        y_att = _dot_nt(att_ref[...], wao_ref[...])
        y_rec = _dot(yr_ref[...], wro_ref[...])
        mixed = (_sigmoid(ga_ref[...]) * y_att + _sigmoid(gr_ref[...]) * y_rec).astype(BF16)
        mixed_ref[...] = mixed
        x1_ref[...] = x_ref[...] + _dot(mixed, wo_ref[...])

    return pl.pallas_call(
        body, name="mix_fwd",
        out_shape=(jax.ShapeDtypeStruct((T, D), F32), jax.ShapeDtypeStruct((T, D), BF16)),
        grid=(T // TM_MIX,),
        in_specs=[tok(D), tok(D_ATT), tok(D_REC), tok(D, 0), tok(D, 1),
                  full((D, D_ATT)), full((D_REC, D)), full((D, D))],
        out_specs=(tok(D), tok(D)),
        compiler_params=_params(dimension_semantics=("parallel",)),
    )(x, att, yrec, gg, gg, w_att_o_t, w_rec_o, w_out)


def _mix_bwd(dx1, att, yrec, gg, w_att_o_t, w_rec_o, w_out):
    tok, full = _mix_specs()

    def body(dx_ref, att_ref, yr_ref, ga_ref, gr_ref, wao_ref, wro_ref, wo_ref,
             dgg_ref, dya_ref, dyr_ref, datt_ref, dyrp_ref):
        dmixed = _dot_nt(dx_ref[...].astype(BF16), wo_ref[...])
        y_att = _dot_nt(att_ref[...], wao_ref[...])
        y_rec = _dot(yr_ref[...], wro_ref[...])
        sa = _sigmoid(ga_ref[...])
        sr = _sigmoid(gr_ref[...])
        dgg_ref[0] = (dmixed * y_att * sa * (1.0 - sa)).astype(BF16)
        dgg_ref[1] = (dmixed * y_rec * sr * (1.0 - sr)).astype(BF16)
        dya = (dmixed * sa).astype(BF16)
        dyr = (dmixed * sr).astype(BF16)
        dya_ref[...] = dya
        dyr_ref[...] = dyr
        datt_ref[...] = _dot(dya, wao_ref[...]).astype(BF16)
        dyrp_ref[...] = _dot_nt(dyr, wro_ref[...])

    return pl.pallas_call(
        body, name="mix_bwd",
        out_shape=(jax.ShapeDtypeStruct((2, T, D), BF16),
                   jax.ShapeDtypeStruct((T, D), BF16), jax.ShapeDtypeStruct((T, D), BF16),
                   jax.ShapeDtypeStruct((T, D_ATT), BF16), jax.ShapeDtypeStruct((T, D_REC), F32)),
        grid=(T // TM_MIX,),
        in_specs=[tok(D), tok(D_ATT), tok(D_REC), tok(D, 0), tok(D, 1),
                  full((D, D_ATT)), full((D_REC, D)), full((D, D))],
        out_specs=(pl.BlockSpec((2, TM_MIX, D), lambda i: (0, i, 0)),
                   tok(D), tok(D), tok(D_ATT), tok(D_REC)),
        compiler_params=_params(dimension_semantics=("parallel",)),
    )(dx1, att, yrec, gg, gg, w_att_o_t, w_rec_o, w_out)


TM_FFN = 256
FF_CHUNK = 1024


def _ffn_loss(x1, target, g2, gf, w_ff1_t, w_ff2):
    n_chunks = D_FF // FF_CHUNK

    def body(x1_ref, tg_ref, g2_ref, gf_ref, w1_hbm, w2_hbm,
             loss_ref, dx1_ref, h2_ref, act_ref, dpre_ref, dx2_ref, dg2_ref, dgf_ref,
             w1, w2, relu_s):
        i = pl.program_id(0)

        @pl.when(i == 0)
        def _():
            pltpu.sync_copy(w1_hbm, w1)
            pltpu.sync_copy(w2_hbm, w2)
            loss_ref[...] = jnp.zeros_like(loss_ref)
            dg2_ref[...] = jnp.zeros_like(dg2_ref)
            dgf_ref[...] = jnp.zeros_like(dgf_ref)

        x1v = x1_ref[...]
        r2 = lax.rsqrt(jnp.mean(x1v * x1v, axis=-1, keepdims=True) + EPS)
        xh2 = x1v * r2
        h2 = (xh2 * g2_ref[...]).astype(BF16)
        h2_ref[...] = h2
        x2 = x1v
        for c in range(n_chunks):
            ff = slice(c * FF_CHUNK, (c + 1) * FF_CHUNK)
            rl = jnp.maximum(_dot_nt(h2, w1[ff, :]), 0.0)
            relu_s[:, ff] = rl
            act = (rl * rl).astype(BF16)
            act_ref[:, ff] = act
            x2 = x2 + _dot(act, w2[ff, :])
        r3 = lax.rsqrt(jnp.mean(x2 * x2, axis=-1, keepdims=True) + EPS)
        xh3 = x2 * r3
        err = xh3 * gf_ref[...] - tg_ref[...]
        loss_ref[...] += 0.5 * jnp.sum(jnp.mean(err * err, axis=-1, keepdims=True))
        dy = err * (1.0 / D)
        dgf_ref[...] += jnp.sum(dy * xh3, axis=0, keepdims=True)
        dx2 = _rms_bwd(dy, xh3, r3, gf_ref[...])
        dx2_16 = dx2.astype(BF16)
        dx2_ref[...] = dx2_16
        dh2 = jnp.zeros((TM_FFN, D), F32)
        for c in range(n_chunks):
            ff = slice(c * FF_CHUNK, (c + 1) * FF_CHUNK)
            dpre = (_dot_nt(dx2_16, w2[ff, :]) * (2.0 * relu_s[:, ff])).astype(BF16)
            dpre_ref[:, ff] = dpre
            dh2 = dh2 + _dot(dpre, w1[ff, :])
        dg2_ref[...] += jnp.sum(dh2 * xh2, axis=0, keepdims=True)
        dx1_ref[...] = dx2 + _rms_bwd(dh2, xh2, r2, g2_ref[...])

    tok = lambda width: pl.BlockSpec((TM_FFN, width), lambda i: (i, 0))
    vec = pl.BlockSpec((1, D), lambda i: (0, 0))
    hbm = pl.BlockSpec(memory_space=pl.ANY)
    return pl.pallas_call(
        body, name="ffn_loss",
        out_shape=(jax.ShapeDtypeStruct((8, 128), F32), jax.ShapeDtypeStruct((T, D), F32),
                   jax.ShapeDtypeStruct((T, D), BF16), jax.ShapeDtypeStruct((T, D_FF), BF16),
                   jax.ShapeDtypeStruct((T, D_FF), BF16), jax.ShapeDtypeStruct((T, D), BF16),
                   jax.ShapeDtypeStruct((1, D), F32), jax.ShapeDtypeStruct((1, D), F32)),
        grid=(T // TM_FFN,),
        in_specs=[tok(D), tok(D), vec, vec, hbm, hbm],
        out_specs=(pl.BlockSpec((8, 128), lambda i: (0, 0)), tok(D), tok(D), tok(D_FF), tok(D_FF), tok(D),
                   vec, vec),
        scratch_shapes=[pltpu.VMEM((D_FF, D), BF16), pltpu.VMEM((D_FF, D), BF16),
                        pltpu.VMEM((TM_FFN, D_FF), F32)],
        compiler_params=_params(dimension_semantics=("arbitrary",)),
    )(x1, target, g2, gf, w_ff1_t, w_ff2)


def _local_step(x, target, p, late_weights, reduce_early):
    bias = _rpb_rows(p["rpb"])
    pairs = lambda w: w.reshape(2, N_CG, CG, REC_BLOCK)
    w_a, w_i = pairs(p["w_rg_a"]), pairs(p["w_rg_i"])
    rec_params = (p["conv_w"], p["conv_b"], w_a, p["b_rg_a"], w_i, p["b_rg_i"], p["lru_lambda"])

    qkv, uy, gg, h = _in_proj(x, p["ln1_g"], p["w_in_t"], p["b_in"])
    att = _att_fwd(qkv, bias)
    hf, hb, yrec, am = _rec_fwd(uy, *rec_params)
    p = {**p, **late_weights(yrec, 0)}
    x1, mixed = _mix_fwd(x, att, yrec, gg, p["w_att_o_t"], p["w_rec_o"], p["w_out"])
    p = {**p, **late_weights(x1, 1)}
    loss8, dx1, h2, act, dpre, dx2, g_ln2, g_lnf = _ffn_loss(
        x1, target, p["ln2_g"], p["lnf_g"], p["w_ff1_t"], p["w_ff2"])

    dgg, dya, dyr, datt, dyrp = _mix_bwd(dx1, att, yrec, gg, p["w_att_o_t"], p["w_rec_o"], p["w_out"])
    duy, g_cw, g_cb, g_wa, g_ba, g_wi, g_bi, g_lam = _rec_bwd(uy, hf, hb, am, dyrp, *rec_params)
    blocks = lambda g: g.reshape(2, N_REC_BLOCKS, REC_BLOCK, REC_BLOCK)
    grads = {
        "w_att_o_t": _matmul(dya, att, "tn", BF16, "g_w_att_o"),
        "conv_w": g_cw, "conv_b": g_cb, "w_rg_a": blocks(g_wa), "b_rg_a": g_ba,
        "w_rg_i": blocks(g_wi), "b_rg_i": g_bi, "lru_lambda": g_lam,
        "w_rec_o": _matmul(yrec, dyr, "tn", BF16, "g_w_rec_o"),
        "w_out": _matmul(mixed, dx1, "tn", BF16, "g_w_out"),
        "ln2_g": g_ln2,
        "w_ff1_t": _matmul(dpre, h2, "tn", BF16, "g_w_ff1"),
        "w_ff2": _matmul(act, dx2, "tn", BF16, "g_w_ff2"),
        "lnf_g": g_lnf,
    }
    dqkv, gbias = _att_bwd(qkv, bias, datt, reduce_early(grads))
    dz = (dqkv, duy, dgg)
    grad_x, g_ln1 = _dh_norm1_bwd(dz, p["w_in_t"], x, p["ln1_g"], dx1)
    g_w_in_t, g_b_in = _grad_w_in(dz, h)
    grads.update(ln1_g=g_ln1, w_in_t=g_w_in_t, b_in=g_b_in, rpb=_rpb_fold(gbias))
    return loss8[0:1, 0:1], grad_x, grads


MESH_ID = pl.DeviceIdType.MESH
ANY = pl.BlockSpec(memory_space=pl.ANY)

CHAN_BLOCK_ROWS = 32
GATE_ROWS = 2 * 2 * N_REC_BLOCKS * REC_BLOCK * REC_BLOCK // (N_DEV * D)
SECTIONS = (("w_in_t", 704, D), ("w_rec_o", 128, D), ("w_out", 128, D), ("w_ff1_t", 512, D),
            ("w_ff2", 512, D), ("chan", CHAN_BLOCK_ROWS, D), ("w_att_o_t", 128, D_ATT),
            ("gates", GATE_ROWS, D))
N_SEC = len(SECTIONS)
N_CHAN_ROWS = 10
CHAN = (("conv_w", 4), ("b_rg_a", 2), ("b_rg_i", 2), ("lru_lambda", 2))


def _position():
    return lax.axis_index("x"), lax.axis_index("y"), lax.axis_index("c")


def _other_chips(x, y):
    return [(1 - x, y), (x, 1 - y), (1 - x, 1 - y)]


PASS_ON_IDS, PAIR_EARLY_ID, PAIR_LATE_ID = (1, 4), 2, 3


def _pair_handshake(x, y, c):
    barrier = pltpu.get_barrier_semaphore()
    pl.semaphore_signal(barrier, inc=1, device_id=(x, y, 1 - c), device_id_type=MESH_ID)
    pl.semaphore_wait(barrier, 1)


def _block_of(ref, dev, rows):
    return ref.at[pl.ds(pl.multiple_of(dev * rows, 16), rows)]


def _all_gather(shards, name):
    ns = len(shards)

    def body(*refs):
        x_refs, out_refs, done_ref = refs[:ns], refs[ns:2 * ns], refs[2 * ns]
        send_sems, recv_sems, local_sems = refs[2 * ns + 1:]
        done_ref[0, 0] = 0.0
        x, y, c = _position()
        me, sibling = (x, y, c), (x, y, 1 - c)
        x_nbr, y_nbr, diagonal = _other_chips(x, y)
        north = c == 1
        relay_from = (jnp.where(north, x_nbr[0], y_nbr[0]), jnp.where(north, x_nbr[1], y_nbr[1]))
        relay_to = (jnp.where(north, y_nbr[0], x_nbr[0]), jnp.where(north, y_nbr[1], x_nbr[1]))

        def rows(s, px, py, pc):
            return _block_of(out_refs[s], 4 * px + 2 * py + pc, shards[s].shape[0])

        def copy(k, s, block, to, from_shard=False):
            return pltpu.make_async_remote_copy(
                src_ref=x_refs[s] if from_shard else rows(s, *block), dst_ref=rows(s, *block),
                send_sem=send_sems.at[k * ns + s], recv_sem=recv_sems.at[k * ns + s],
                device_id=to, device_id_type=MESH_ID)

        sections = range(ns)
        mine = [pltpu.make_async_copy(x_refs[s], rows(s, *me), local_sems.at[s]) for s in sections]
        sent = [copy(k, s, me, to, True) for k, to in enumerate((sibling, (*x_nbr, c), (*y_nbr, c)))
                for s in sections]
        for cp in mine + sent:
            cp.start()
        for s in sections:
            copy(1, s, (*x_nbr, c), me).wait_recv()
            copy(2, s, (*y_nbr, c), me).wait_recv()
            sent += [copy(3, s, (*relay_from, c), (*relay_to, c)),
                     copy(4, s, (*x_nbr, c), sibling), copy(5, s, (*y_nbr, c), sibling)]
            for cp in sent[-3:]:
                cp.start()
        for s in sections:
            copy(3, s, (*diagonal, c), me).wait_recv()
            sent.append(copy(6, s, (*diagonal, c), sibling))
            sent[-1].start()
        for s in sections:
            copy(0, s, sibling, me).wait_recv()
            for k, chip in ((4, x_nbr), (5, y_nbr), (6, diagonal)):
                copy(k, s, (*chip, 1 - c), me).wait_recv()
        for cp in sent:
            cp.wait_send()
        for cp in mine:
            cp.wait()

    return pl.pallas_call(
        body, name=name,
        out_shape=tuple(jax.ShapeDtypeStruct((N_DEV * s.shape[0], s.shape[1]), s.dtype) for s in shards)
        + (jax.ShapeDtypeStruct((1, 1), F32),),
        in_specs=[ANY] * ns,
        out_specs=(ANY,) * ns + (pl.BlockSpec(memory_space=pltpu.SMEM),),
        scratch_shapes=[pltpu.SemaphoreType.DMA((7 * ns,)), pltpu.SemaphoreType.DMA((7 * ns,)),
                        pltpu.SemaphoreType.DMA((ns,))],
    )(*shards)


HBM = pl.BlockSpec(memory_space=pltpu.HBM)
SEM = pl.BlockSpec(memory_space=pltpu.SEMAPHORE)
EFFECT = pltpu.SideEffectType.DATAFLOW_SIDE_EFFECTING


def _in_hbm(a):
    return pltpu.with_memory_space_constraint(a, pltpu.HBM)


def _first_hop_copies(shards, x_refs, zones, send_sems, recv_sems):
    ns = len(shards)
    x, y, c = _position()
    targets = [(x, y, 1 - c)] + [(cx, cy, c) for cx, cy in _other_chips(x, y)]
    return [pltpu.make_async_remote_copy(
        src_ref=x_refs[s], dst_ref=_block_of(zones[s], 4 * x + 2 * y + c, shards[s].shape[0]),
        send_sem=send_sems.at[k * ns + s], recv_sem=recv_sems.at[k * ns + s],
        device_id=to, device_id_type=MESH_ID)
        for k, to in enumerate(targets) for s in range(ns)]


def _after_all(arrays, name):
    def body(*refs):
        refs[-1][...] = jnp.zeros_like(refs[-1])

    return pl.pallas_call(
        body, name=name,
        out_shape=jax.ShapeDtypeStruct((8, LANES), F32),
        in_specs=[pl.BlockSpec(memory_space=pl.ANY)] * len(arrays),
        out_specs=pl.BlockSpec(memory_space=pltpu.VMEM),
    )(*arrays)


def _own_blocks_placed(shards, after):
    ns = len(shards)
    x, y, c = _position()
    me = jnp.reshape(4 * x + 2 * y + c, (1,)).astype(jnp.int32)
    shards = [*shards[:-1], shards[-1] + after.astype(shards[-1].dtype)]

    def body(me_ref, *refs):
        for s in range(ns):
            refs[ns + s][...] = refs[s][...]

    return pl.pallas_call(
        body, name="own_blocks_placed",
        out_shape=tuple(jax.ShapeDtypeStruct((N_DEV * s.shape[0], s.shape[1]), s.dtype) for s in shards),
        grid_spec=pltpu.PrefetchScalarGridSpec(
            num_scalar_prefetch=1, grid=(1,),
            in_specs=[pl.BlockSpec(s.shape, lambda i, me: (0, 0)) for s in shards],
            out_specs=tuple(pl.BlockSpec(s.shape, lambda i, me: (me[0], 0)) for s in shards)),
        compiler_params=_params(dimension_semantics=("arbitrary",)),
    )(me, *shards)


def _gather_start(shards, after, name):
    ns = len(shards)
    zones = _own_blocks_placed(shards, after)

    def body(*refs):
        for cp in _first_hop_copies(shards, refs[:ns], refs[ns:2 * ns], refs[2 * ns], refs[2 * ns + 1]):
            cp.start()
        refs[-1][...] = jnp.zeros_like(refs[-1])

    out = pl.pallas_call(
        body, name=name,
        out_shape=(pltpu.SemaphoreType.DMA((4 * ns,)), pltpu.SemaphoreType.DMA((4 * ns,)),
                   *[pltpu.HBM(a.shape, a.dtype) for a in (*shards, *zones)],
                   jax.ShapeDtypeStruct((8, LANES), F32)),
        in_specs=[HBM] * (2 * ns),
        out_specs=(SEM, SEM, *[HBM] * (2 * ns), pl.BlockSpec(memory_space=pltpu.VMEM)),
        input_output_aliases={i: 2 + i for i in range(2 * ns)},
        compiler_params=pltpu.CompilerParams(has_side_effects=EFFECT),
    )(*[_in_hbm(a) for a in shards], *[_in_hbm(a) for a in zones])
    return out[0], out[1], out[2:2 + ns], out[2 + ns:2 + 2 * ns], out[-1]


def _gather_wait(send_sems, recv_sems, shards, zones, which, after, name):
    ns = len(shards)

    def body(*refs):
        copies = _first_hop_copies(shards, refs[:ns], refs[ns:2 * ns], refs[2 * ns], refs[2 * ns + 1])
        for i, cp in enumerate(copies):
            if i % ns in which:
                cp.wait_send()
                cp.wait_recv()

    out = pl.pallas_call(
        body, name=name,
        out_shape=tuple(pltpu.HBM(a.shape, a.dtype) for a in (*shards, *zones)),
        in_specs=[HBM] * (2 * ns) + [SEM, SEM, ANY],
        out_specs=(HBM,) * (2 * ns),
        input_output_aliases={i: i for i in range(2 * ns)},
        compiler_params=pltpu.CompilerParams(has_side_effects=EFFECT),
    )(*shards, *zones, send_sems, recv_sems, after)
    return out[:ns], out[ns:]


def _gather_pass_on(rows, zones, barrier_id, name):
    ns = len(zones)

    def body(*refs):
        in_refs, out_refs = refs[:ns], refs[ns:2 * ns]
        send_sems, recv_sems = refs[2 * ns:]
        x, y, c = _position()
        _pair_handshake(x, y, c)
        copies = [pltpu.make_async_remote_copy(
            src_ref=_block_of(in_refs[s], 4 * cx + 2 * cy + c, rows[s]),
            dst_ref=_block_of(out_refs[s], 4 * cx + 2 * cy + c, rows[s]),
            send_sem=send_sems.at[j * ns + s], recv_sem=recv_sems.at[j * ns + s],
            device_id=(x, y, 1 - c), device_id_type=MESH_ID)
            for j, (cx, cy) in enumerate(_other_chips(x, y)) for s in range(ns)]
        for cp in copies:
            cp.start()
        for cp in copies:
            cp.wait_recv()
        for cp in copies:
            cp.wait_send()

    return pl.pallas_call(
        body, name=name,
        out_shape=tuple(jax.ShapeDtypeStruct(z.shape, z.dtype) for z in zones),
        in_specs=[ANY] * ns, out_specs=(ANY,) * ns,
        input_output_aliases={i: i for i in range(ns)},
        scratch_shapes=[pltpu.SemaphoreType.DMA((3 * ns,)), pltpu.SemaphoreType.DMA((3 * ns,))],
        compiler_params=pltpu.CompilerParams(collective_id=barrier_id),
    )(*zones)


def _pair_copies(sections, g_refs, land, send_sems, recv_sems):
    ns = len(sections)
    x, y, c = _position()
    return [pltpu.make_async_remote_copy(
        src_ref=_block_of(g_refs[s], 2 * k + 1 - c, rows), dst_ref=land[s].at[k],
        send_sem=send_sems.at[k * ns + s], recv_sem=recv_sems.at[k * ns + s],
        device_id=(x, y, 1 - c), device_id_type=MESH_ID)
        for k in range(N_CHIPS) for s, (_, rows, _) in enumerate(sections)]


def _pair_exchange_start(sections, grads, barrier_id, name):
    ns = len(sections)

    def body(*refs):
        _pair_handshake(*_position())
        for cp in _pair_copies(sections, refs[:ns], refs[ns:2 * ns], refs[2 * ns], refs[2 * ns + 1]):
            cp.start()
        refs[-1][...] = jnp.zeros_like(refs[-1])

    zones = [lax.empty((N_CHIPS, rows, cols), BF16) for _, rows, cols in sections]
    n = N_CHIPS * ns
    out = pl.pallas_call(
        body, name=name,
        out_shape=(pltpu.SemaphoreType.DMA((n,)), pltpu.SemaphoreType.DMA((n,)),
                   *[pltpu.HBM(a.shape, a.dtype) for a in (*grads, *zones)],
                   jax.ShapeDtypeStruct((8, LANES), F32)),
        in_specs=[HBM] * (2 * ns),
        out_specs=(SEM, SEM, *[HBM] * (2 * ns), pl.BlockSpec(memory_space=pltpu.VMEM)),
        input_output_aliases={i: 2 + i for i in range(2 * ns)},
        compiler_params=pltpu.CompilerParams(has_side_effects=EFFECT, collective_id=barrier_id),
    )(*[_in_hbm(a) for a in grads], *[_in_hbm(a) for a in zones])
    return out[0], out[1], out[2:2 + ns], out[2 + ns:2 + 2 * ns], out[-1]


def _pair_exchange_wait(sections, send_sems, recv_sems, grads, zones, after, name):
    ns = len(sections)

    def body(*refs):
        for cp in _pair_copies(sections, refs[:ns], refs[ns:2 * ns], refs[2 * ns], refs[2 * ns + 1]):
            cp.wait_send()
            cp.wait_recv()

    out = pl.pallas_call(
        body, name=name,
        out_shape=tuple(pltpu.HBM(a.shape, a.dtype) for a in (*grads, *zones)),
        in_specs=[HBM] * (2 * ns) + [SEM, SEM, ANY],
        out_specs=(HBM,) * (2 * ns),
        input_output_aliases={i: i for i in range(2 * ns)},
        compiler_params=pltpu.CompilerParams(has_side_effects=EFFECT),
    )(*grads, *zones, send_sems, recv_sems, after)
    return out[:ns], out[ns:]


def _pair_add(sections, grads, got, core, name):
    ns = len(sections)

    def body(core_ref, *refs):
        g_refs, got_refs, p_refs = refs[:ns], refs[ns:2 * ns], refs[2 * ns:]
        for s in range(ns):
            p_refs[s][0] = (g_refs[s][...].astype(F32) + got_refs[s][0].astype(F32)).astype(BF16)

    slot = [pl.BlockSpec((1, rows, cols), lambda k, c: (k, 0, 0)) for _, rows, cols in sections]
    return pl.pallas_call(
        body, name=name,
        out_shape=tuple(jax.ShapeDtypeStruct((N_CHIPS, rows, cols), BF16) for _, rows, cols in sections),
        grid_spec=pltpu.PrefetchScalarGridSpec(
            num_scalar_prefetch=1, grid=(N_CHIPS,),
            in_specs=[pl.BlockSpec((rows, cols), lambda k, c: (2 * k + c[0], 0)) for _, rows, cols in sections]
            + slot,
            out_specs=tuple(slot)),
        compiler_params=_params(dimension_semantics=("parallel",)),
    )(core, *grads, *got)


def _chip_copies(sections, p_refs, land, send_sems, recv_sems):
    ns = len(sections)
    x, y, c = _position()
    return [pltpu.make_async_remote_copy(
        src_ref=p_refs[s].at[2 * cx + cy], dst_ref=land[s].at[j],
        send_sem=send_sems.at[j * ns + s], recv_sem=recv_sems.at[j * ns + s],
        device_id=(cx, cy, c), device_id_type=MESH_ID)
        for j, (cx, cy) in enumerate(_other_chips(x, y)) for s in range(ns)]


def _chip_exchange(sections, parts, name):
    ns = len(sections)

    def body(*refs):
        copies = _chip_copies(sections, refs[:ns], refs[ns:2 * ns], *refs[2 * ns:])
        for cp in copies:
            cp.start()
        for cp in copies:
            cp.wait_recv()
        for cp in copies:
            cp.wait_send()

    n = 3 * ns
    return pl.pallas_call(
        body, name=name,
        out_shape=tuple(jax.ShapeDtypeStruct((3, rows, cols), BF16) for _, rows, cols in sections),
        in_specs=[ANY] * ns, out_specs=(ANY,) * ns,
        scratch_shapes=[pltpu.SemaphoreType.DMA((n,)), pltpu.SemaphoreType.DMA((n,))],
    )(*parts)


def _chip_exchange_start(sections, parts, name):
    ns = len(sections)

    def body(*refs):
        p_refs, land = refs[:ns], refs[ns:2 * ns]
        send_sems, recv_sems = refs[2 * ns], refs[2 * ns + 1]
        token = refs[-1]
        for cp in _chip_copies(sections, p_refs, land, send_sems, recv_sems):
            cp.start()
        token[...] = jnp.zeros_like(token)

    zones = [lax.empty((3, rows, cols), BF16) for _, rows, cols in sections]
    out = pl.pallas_call(
        body, name=name,
        out_shape=(pltpu.SemaphoreType.DMA((3 * ns,)), pltpu.SemaphoreType.DMA((3 * ns,)),
                   *[pltpu.HBM(a.shape, a.dtype) for a in parts], *[pltpu.HBM(a.shape, a.dtype) for a in zones],
                   jax.ShapeDtypeStruct((8, LANES), F32)),
        in_specs=[HBM] * (2 * ns),
        out_specs=(SEM, SEM, *[HBM] * (2 * ns), pl.BlockSpec(memory_space=pltpu.VMEM)),
        input_output_aliases={i: 2 + i for i in range(2 * ns)},
        compiler_params=pltpu.CompilerParams(has_side_effects=EFFECT),
    )(*[_in_hbm(a) for a in parts], *[_in_hbm(a) for a in zones])
    return out[0], out[1], out[2:2 + ns], out[2 + ns:2 + 2 * ns], out[-1]


def _chip_exchange_wait(sections, send_sems, recv_sems, parts, zones, after, name):
    ns = len(sections)

    def body(*refs):
        p_refs, land = refs[:ns], refs[ns:2 * ns]
        for cp in _chip_copies(sections, p_refs, land, refs[2 * ns], refs[2 * ns + 1]):
            cp.wait_send()
            cp.wait_recv()

    out = pl.pallas_call(
        body, name=name,
        out_shape=tuple(pltpu.HBM(a.shape, a.dtype) for a in (*parts, *zones)),
        in_specs=[HBM] * (2 * ns) + [SEM, SEM, ANY],
        out_specs=(HBM,) * (2 * ns),
        input_output_aliases={i: i for i in range(2 * ns)},
        compiler_params=pltpu.CompilerParams(has_side_effects=EFFECT),
    )(*parts, *zones, send_sems, recv_sems, after)
    return out[:ns], out[ns:]


def _grad_finish(sections, parts, far, chip, name):
    ns = len(sections)

    def body(chip_ref, *refs):
        p_refs, b_refs, g_refs = refs[:ns], refs[ns:2 * ns], refs[2 * ns:]
        for s in range(ns):
            g = p_refs[s][0].astype(F32)
            for j in range(3):
                g = g + b_refs[s][j].astype(F32)
            g_refs[s][...] = g

    half = [(rows // 2, cols) for _, rows, cols in sections]
    return pl.pallas_call(
        body, name=name,
        out_shape=tuple(jax.ShapeDtypeStruct((rows, cols), F32) for _, rows, cols in sections),
        grid_spec=pltpu.PrefetchScalarGridSpec(
            num_scalar_prefetch=1, grid=(2,),
            in_specs=[pl.BlockSpec((1, r, c), lambda i, chip: (chip[0], i, 0)) for r, c in half]
            + [pl.BlockSpec((3, r, c), lambda i, chip: (0, i, 0)) for r, c in half],
            out_specs=tuple(pl.BlockSpec((r, c), lambda i, chip: (i, 0)) for r, c in half)),
        compiler_params=_params(dimension_semantics=("parallel",)),
    )(chip, *parts, *far)


def _sum_devices(parts, rows, name):
    cols = parts.shape[1]
    tr = rows // 2

    def body(*refs):
        s = refs[0][...].astype(F32)
        for d in range(1, N_DEV):
            s = s + refs[d][...].astype(F32)
        refs[N_DEV][...] = s

    return pl.pallas_call(
        body, name=name,
        out_shape=jax.ShapeDtypeStruct((rows, cols), F32),
        grid=(2,),
        in_specs=[pl.BlockSpec((tr, cols), lambda i, d=d: (2 * d + i, 0)) for d in range(N_DEV)],
        out_specs=pl.BlockSpec((tr, cols), lambda i: (i, 0)),
        compiler_params=_params(dimension_semantics=("parallel",)),
    )(*([parts] * N_DEV))


def _adamw_step(w_ref, g_ref, m_ref, v_ref, d_ref, nm_ref, nv_ref):
    c1 = 1.0 / (1.0 - ADAM_B1 ** ADAM_STEP)
    c2 = 1.0 / (1.0 - ADAM_B2 ** ADAM_STEP)
    gv = g_ref[...]
    nm = ADAM_B1 * m_ref[...] + (1.0 - ADAM_B1) * gv
    nv = ADAM_B2 * v_ref[...] + (1.0 - ADAM_B2) * (gv * gv)
    nm_ref[...] = nm
    nv_ref[...] = nv
    d_ref[...] = (-ADAM_LR) * ((nm * c1) / (jnp.sqrt(nv * c2) + ADAM_EPS) + ADAM_WD * w_ref[...])


def _adamw_small(params, name):
    n = len(params)

    def body(*refs):
        for k in range(n):
            _adamw_step(*refs[4 * k:4 * k + 4], *refs[4 * n + 3 * k:4 * n + 3 * k + 3])

    out = pl.pallas_call(
        body, name=name,
        out_shape=tuple(jax.ShapeDtypeStruct(p[0].shape, F32) for p in params for _ in range(3)),
    )(*[a for p in params for a in p])
    return [out[3 * k:3 * k + 3] for k in range(n)]


def _adamw(w, g, m, v, name):
    rows, cols = w.shape
    tr = rows
    while tr * cols * 4 > (1 << 20) and tr % 16 == 0:
        tr //= 2

    def body(*refs):
        _adamw_step(*refs)

    spec = pl.BlockSpec((tr, cols), lambda i: (i, 0))
    shape = jax.ShapeDtypeStruct((rows, cols), F32)
    return pl.pallas_call(
        body, name=name,
        out_shape=(shape, shape, shape),
        grid=(rows // tr,),
        in_specs=[spec] * 4, out_specs=(spec,) * 3,
        compiler_params=_params(dimension_semantics=("parallel",)),
    )(w, g, m, v)


NAMES = ("ln1_g", "w_in", "b_in", "rpb", "w_att_o", "conv_w", "conv_b", "w_rg_a", "b_rg_a", "w_rg_i",
         "b_rg_i", "lru_lambda", "w_rec_o", "w_out", "ln2_g", "w_ff1", "w_ff2", "lnf_g")
TRANSPOSED = {"w_in": "w_in_t", "w_att_o": "w_att_o_t", "w_ff1": "w_ff1_t"}
ROW_SHARDED = ("w_rec_o", "w_out", "w_ff2")
REPLICATED = (("ln1_g", (1, D)), ("b_in", (1, D_IN)), ("rpb", (N_HEADS * N_RPB_R, N_RPB_C)),
              ("conv_b", (1, D_REC)), ("w_rg_a", (2 * N_REC_BLOCKS * REC_BLOCK, REC_BLOCK)),
              ("w_rg_i", (2 * N_REC_BLOCKS * REC_BLOCK, REC_BLOCK)), ("ln2_g", (1, D)), ("lnf_g", (1, D)))
GATE_BLOCKS = ("w_rg_a", "w_rg_i")
SMALL_ROWS = 112


def _chan_bits(vectors):
    chan = jnp.concatenate(vectors, axis=0)
    bits = lax.bitcast_convert_type(chan, BF16).reshape(-1)
    return jnp.pad(bits, (0, CHAN_BLOCK_ROWS * D - bits.shape[0])).reshape(CHAN_BLOCK_ROWS, D)


def _chan_from_bits(gathered):
    bits = gathered.reshape(N_DEV, CHAN_BLOCK_ROWS * D)[:, :2 * N_CHAN_ROWS * LANES]
    chan = lax.bitcast_convert_type(bits.reshape(N_DEV, N_CHAN_ROWS, LANES, 2), F32)
    return chan.transpose(1, 0, 2).reshape(N_CHAN_ROWS, D)


def kernel(x, ln1_g, w_in, b_in, rpb, w_att_o, conv_w, conv_b, w_rg_a, b_rg_a, w_rg_i, b_rg_i, lru_lambda, w_rec_o, w_out, ln2_g, w_ff1, w_ff2, lnf_g, loss_target, m_ln1_g, m_w_in, m_b_in, m_rpb, m_w_att_o, m_conv_w, m_conv_b, m_w_rg_a, m_b_rg_a, m_w_rg_i, m_b_rg_i, m_lru_lambda, m_w_rec_o, m_w_out, m_ln2_g, m_w_ff1, m_w_ff2, m_lnf_g, v_ln1_g, v_w_in, v_b_in, v_rpb, v_w_att_o, v_conv_w, v_conv_b, v_w_rg_a, v_b_rg_a, v_w_rg_i, v_b_rg_i, v_lru_lambda, v_w_rec_o, v_w_out, v_ln2_g, v_w_ff1, v_w_ff2, v_lnf_g):
    w = dict(zip(NAMES, (ln1_g, w_in, b_in, rpb, w_att_o, conv_w, conv_b, w_rg_a, b_rg_a, w_rg_i,
                         b_rg_i, lru_lambda, w_rec_o, w_out, ln2_g, w_ff1, w_ff2, lnf_g)))
    m = dict(zip(NAMES, (m_ln1_g, m_w_in, m_b_in, m_rpb, m_w_att_o, m_conv_w, m_conv_b, m_w_rg_a,
                         m_b_rg_a, m_w_rg_i, m_b_rg_i, m_lru_lambda, m_w_rec_o, m_w_out, m_ln2_g,
                         m_w_ff1, m_w_ff2, m_lnf_g)))
    v = dict(zip(NAMES, (v_ln1_g, v_w_in, v_b_in, v_rpb, v_w_att_o, v_conv_w, v_conv_b, v_w_rg_a,
                         v_b_rg_a, v_w_rg_i, v_b_rg_i, v_lru_lambda, v_w_rec_o, v_w_out, v_ln2_g,
                         v_w_ff1, v_w_ff2, v_lnf_g)))
    xi, yi, ci = _position()

    shard = {t: w[n][0].T.astype(BF16) for n, t in TRANSPOSED.items()}
    shard.update({n: w[n][0].astype(BF16) for n in ROW_SHARDED})
    shard["chan"] = _chan_bits([w[n][0] for n, _ in CHAN])
    first, later = ("w_in_t", "chan"), ("w_rec_o", "w_out", "w_att_o_t", "w_ff1_t", "w_ff2")
    *gathered, done = _all_gather([shard[n] for n in first], "weight_all_gather")
    p = dict(zip(first, gathered))
    send_sems, recv_sems, sent, zones, token = _gather_start([shard[n] for n in later], done,
                                                             "weight_gather_start")

    travelling = {"shards": sent, "zones": zones}
    stages = (("w_rec_o", "w_out", "w_att_o_t"), ("w_ff1_t", "w_ff2"))

    def late_weights(after, stage):
        which = [later.index(n) for n in stages[stage]]
        travelling["shards"], travelling["zones"] = _gather_wait(
            send_sems, recv_sems, travelling["shards"], travelling["zones"], which, after,
            "weight_gather_wait_%d" % stage)
        return dict(zip(stages[stage], _gather_pass_on(
            [shard[n].shape[0] for n in stages[stage]], [travelling["zones"][i] for i in which],
            PASS_ON_IDS[stage], "weight_gather_pass_on_%d" % stage)))

    chan = _chan_from_bits(p.pop("chan"))
    r0 = 0
    for n, rows in CHAN:
        p[n] = chan[r0:r0 + rows]
        r0 += rows
    p.update(ln1_g=w["ln1_g"], b_in=w["b_in"] + token[0, 0], rpb=w["rpb"][0], conv_b=w["conv_b"],
             w_rg_a=w["w_rg_a"][0], w_rg_i=w["w_rg_i"][0], ln2_g=w["ln2_g"],
             lnf_g=w["lnf_g"].reshape(1, D))

    core = jnp.reshape(ci, (1,)).astype(jnp.int32)
    chip = jnp.reshape(2 * xi + yi, (1,)).astype(jnp.int32)
    early_sections, late_sections = SECTIONS[1:], SECTIONS[:1]
    in_flight = {}

    def pair_sum_and_send(group, sections, after):
        send_sems, recv_sems, sect, zones, _ = in_flight["pair_" + group]
        sect, got = _pair_exchange_wait(sections, send_sems, recv_sems, sect, zones, after,
                                        "grad_pair_exchange_wait_" + group)
        parts = _pair_add(sections, sect, got, core, "grad_pair_add_" + group)
        in_flight[group] = _chip_exchange_start(sections, parts, "grad_chip_exchange_start_" + group)
        return in_flight[group][-1]

    def reduce_early(grads):
        chan_g = jnp.concatenate([grads[n] for n, _ in CHAN], axis=0)
        chan_g = chan_g.reshape(N_CHAN_ROWS, N_DEV, LANES).transpose(1, 0, 2).astype(BF16)
        chan_g = jnp.pad(chan_g.reshape(N_DEV, -1), ((0, 0), (0, CHAN_BLOCK_ROWS * D - N_CHAN_ROWS * LANES)))
        grads["chan"] = chan_g.reshape(N_DEV * CHAN_BLOCK_ROWS, D)
        grads["gates"] = jnp.concatenate([grads[n].reshape(-1, D) for n in GATE_BLOCKS], axis=0).astype(BF16)
        in_flight["pair_early"] = _pair_exchange_start(
            early_sections, [grads[n] for n, _, _ in early_sections], PAIR_EARLY_ID,
            "grad_pair_exchange_start_early")
        return pair_sum_and_send("early", early_sections, in_flight["pair_early"][-1])[0, 0]

    loss_part, grad_x, grads = _local_step(x[0], loss_target[0], p, late_weights, reduce_early)
    in_flight["pair_late"] = _pair_exchange_start(
        late_sections, [grads[n] for n, _, _ in late_sections], PAIR_LATE_ID, "grad_pair_exchange_start_late")

    def finish(group, sections, after, name):
        send_sems, recv_sems, parts, zones, _ = in_flight[group]
        parts, far = _chip_exchange_wait(sections, send_sems, recv_sems, parts, zones, after,
                                         "grad_chip_exchange_wait_" + name)
        return dict(zip((n for n, _, _ in sections),
                        _grad_finish(sections, parts, far, chip, "grad_finish_" + name)))

    summed = finish("early", early_sections, in_flight["pair_late"][-1], "early")
    started_late = pair_sum_and_send("late", late_sections, summed["gates"])

    flat = jnp.concatenate([grads[n].reshape(-1) for n, _ in REPLICATED if n not in GATE_BLOCKS]
                           + [loss_part.reshape(-1) + started_late[0, 0]])
    n_small = flat.shape[0]
    flat = jnp.pad(flat, (0, SMALL_ROWS * LANES - n_small)).reshape(SMALL_ROWS, LANES)
    small_parts, gate_sum, _ = _all_gather([flat, summed["gates"]], "small_grad_all_gather")
    small = _sum_devices(small_parts, SMALL_ROWS, "small_grad_sum").reshape(-1)
    loss = small[n_small - 1]

    g, delta, new_m, new_v = {}, {}, {}, {}

    def update(n, g2, shape2):
        d2, m2, v2 = _adamw(w[n].reshape(shape2), g2, m[n].reshape(shape2), v[n].reshape(shape2),
                            "adamw_" + n)
        g[n], delta[n], new_m[n], new_v[n] = (a.reshape(w[n].shape) for a in (g2, d2, m2, v2))

    small_params = []
    o = 0
    for n, shape2 in REPLICATED:
        if n in GATE_BLOCKS:
            k, rows = GATE_BLOCKS.index(n), gate_sum.shape[0] // len(GATE_BLOCKS)
            update(n, gate_sum[k * rows:(k + 1) * rows].reshape(shape2), shape2)
        else:
            size = shape2[0] * shape2[1]
            small_params.append((n, small[o:o + size].reshape(shape2), shape2))
            o += size
    chan_back = summed["chan"].reshape(-1)[:N_CHAN_ROWS * LANES].reshape(N_CHAN_ROWS, LANES)
    r0 = 0
    for n, rows in CHAN:
        small_params.append((n, chan_back[r0:r0 + rows], (rows, LANES)))
        r0 += rows
    results = _adamw_small([(w[n].reshape(s2), g2, m[n].reshape(s2), v[n].reshape(s2))
                            for n, g2, s2 in small_params], "adamw_vectors")
    for (n, g2, _), (d2, m2, v2) in zip(small_params, results):
        g[n], delta[n], new_m[n], new_v[n] = (a.reshape(w[n].shape) for a in (g2, d2, m2, v2))

    for n in ROW_SHARDED:
        update(n, summed[n], summed[n].shape)
    for n, t in TRANSPOSED.items():
        if t in summed:
            update(n, summed[t].T, summed[t].shape[::-1])
    summed = finish("late", late_sections, _after_all(list(delta.values()), "updates_done"), "late")
    update("w_in", summed["w_in_t"].T, summed["w_in_t"].shape[::-1])

    return (loss, grad_x[None], *[g[n] for n in NAMES], *[delta[n] for n in NAMES],
            *[new_m[n] for n in NAMES], *[new_v[n] for n in NAMES])
```

```python
import math

import numpy as np
import jax
import jax.numpy as jnp
from jax import lax
from jax.experimental import pallas as pl
from jax.experimental.pallas import tpu as pltpu

F32 = jnp.float32
BF16 = jnp.bfloat16

T = 2048
D = 1024
D_ATT = 512
D_REC = 1024
D_FF = 4096
D_IN = 5632
N_HEADS = 8
DH = 64
GRID_W = 64
ROWS = T // GRID_W
WIN_H = 8
WIN_W = 16
KWIN = WIN_H * GRID_W
N_RPB_R = 2 * WIN_H - 1
N_RPB_C = 2 * WIN_W - 1
N_REC_BLOCKS = 16
REC_BLOCK = 64
CG = 128
N_CG = D_REC // CG
LRU_C = 8.0
EPS = 1e-6
N_DEV = 8
N_CHIPS = 4
LANES = 128

ADAM_LR = 0.001
ADAM_B1 = 0.9
ADAM_B2 = 0.999
ADAM_EPS = 1e-08
ADAM_WD = 0.01
ADAM_STEP = 10

MESH_AXES = ("x", "y", "c")
VMEM_LIMIT = 56 * 1024 * 1024

TILE = 512
DZ_ARRAYS = ((0, 3, 1), (3, 4, 2), (7, 4, 2))
N_DZ_TILES = D_IN // TILE


def _params(**kw):
    return pltpu.CompilerParams(vmem_limit_bytes=VMEM_LIMIT, **kw)


HG = 4
HQ = HG * GRID_W
HC = HG * DH


def _att_tables():
    rq = np.arange(GRID_W)
    kc = np.arange(KWIN) % GRID_W
    win_start = np.clip(rq - WIN_W // 2, 0, GRID_W - WIN_W)
    valid = (kc[None, :] >= win_start[:, None]) & (kc[None, :] < win_start[:, None] + WIN_W)
    same_head = (np.arange(HQ)[:, None] // GRID_W) == (np.arange(HC)[None, :] // DH)
    return valid.astype(np.float32), same_head.astype(np.float32)


def _pair_mask():
    half = np.arange(2 * DH) // DH
    return (half[:, None] == half[None, :]).astype(np.float32)


def _dup_table():
    return np.concatenate([np.eye(REC_BLOCK, dtype=np.float32)] * 2, axis=1)


def _sigmoid(x):
    return 0.5 * jnp.tanh(0.5 * x) + 0.5


def _softplus(x):
    return jnp.maximum(x, 0.0) + jnp.log(1.0 + jnp.exp(-jnp.abs(x)))


def _one_minus_square(log_a, a):
    x = 2.0 * log_a
    series = -x * (1.0 + x * (0.5 + x * (1.0 / 6.0)))
    return jnp.where(x > -0.02, series, 1.0 - a * a)


_GELU_C = math.sqrt(2.0 / math.pi)


def _gelu_and_grad(x):
    x2 = x * x
    inner = _GELU_C * (x + 0.044715 * x * x2)
    t = jnp.tanh(inner)
    g = 0.5 * x * (1.0 + t)
    dg = 0.5 * (1.0 + t) + 0.5 * x * (1.0 - t * t) * _GELU_C * (1.0 + 3.0 * 0.044715 * x2)
    return g, dg


def _dot(a, b):
    return jnp.dot(a, b, preferred_element_type=F32)


def _dot_nt(a, b):
    return lax.dot_general(a, b, (((1,), (1,)), ((), ())), preferred_element_type=F32)


def _dot_tn(a, b):
    return lax.dot_general(a, b, (((0,), (0,)), ((), ())), preferred_element_type=F32)


def _dot_exact(a, b):
    return jnp.dot(a, b, precision=lax.Precision.HIGHEST, preferred_element_type=F32)


def _shift_rows(x, s):
    n = x.shape[0]
    rows = lax.broadcasted_iota(jnp.int32, x.shape, 0)
    y = pltpu.roll(x, s % n, 0)
    if s > 0:
        return jnp.where(rows >= s, y, 0.0)
    return jnp.where(rows < n + s, y, 0.0)


def _rms_bwd(dh, xh, r, g):
    dxh = dh * g
    return r * (dxh - xh * jnp.mean(dxh * xh, axis=-1, keepdims=True))


def _matmul(a, b, mode, out_dtype, name, tm=512, tn=1024, tk=2048):
    if mode == "nn":
        (m, k), (k2, n) = a.shape, b.shape
    elif mode == "nt":
        (m, k), (n, k2) = a.shape, b.shape
    else:
        (k, m), (k2, n) = a.shape, b.shape
    assert k == k2
    tm, tn, tk = min(tm, m), min(tn, n), min(tk, k)
    assert m % tm == 0 and n % tn == 0 and k % tk == 0
    nk = k // tk
    dot = {"nn": _dot, "nt": _dot_nt, "tn": _dot_tn}[mode]

    def body(a_ref, b_ref, o_ref, acc):
        kk = pl.program_id(2)
        part = dot(a_ref[...].astype(BF16), b_ref[...].astype(BF16))
        if nk == 1:
            o_ref[...] = part.astype(out_dtype)
            return

        @pl.when(kk == 0)
        def _():
            acc[...] = part

        @pl.when(kk > 0)
        def _():
            acc[...] += part

        @pl.when(kk == nk - 1)
        def _():
            o_ref[...] = acc[...].astype(out_dtype)

    if mode == "tn":
        a_spec = pl.BlockSpec((tk, tm), lambda i, j, kk: (kk, i))
    else:
        a_spec = pl.BlockSpec((tm, tk), lambda i, j, kk: (i, kk))
    if mode == "nt":
        b_spec = pl.BlockSpec((tn, tk), lambda i, j, kk: (j, kk))
    else:
        b_spec = pl.BlockSpec((tk, tn), lambda i, j, kk: (kk, j))
    return pl.pallas_call(
        body, name=name,
        out_shape=jax.ShapeDtypeStruct((m, n), out_dtype),
        grid=(m // tm, n // tn, nk),
        in_specs=[a_spec, b_spec],
        out_specs=pl.BlockSpec((tm, tn), lambda i, j, kk: (i, j)),
        scratch_shapes=[pltpu.VMEM((tm, tn) if nk > 1 else (8, LANES), F32)],
        compiler_params=_params(dimension_semantics=("parallel", "parallel", "arbitrary")),
    )(a, b)


def _stream_in(w_hbm, w, sems, chunk, first_step):
    copies = [pltpu.make_async_copy(w_hbm.at[pl.ds(k * chunk, chunk)], w.at[pl.ds(k * chunk, chunk)], sems.at[k])
              for k in range(w.shape[0] // chunk)]

    @pl.when(first_step)
    def _():
        for cp in copies:
            cp.start()

    def arrived(row0, rows):
        @pl.when(first_step)
        def _():
            for cp in copies[row0 // chunk:(row0 + rows) // chunk]:
                cp.wait()

    return arrived


def _in_proj(x, g1, w_in_t, b_in):
    tm = 512

    def body(x_ref, g_ref, w_hbm, b_ref, qkv_ref, uy_ref, gg_ref, h_ref, w, sems):
        arrived = _stream_in(w_hbm, w, sems, TILE, pl.program_id(0) == 0)
        xv = x_ref[...]
        r = lax.rsqrt(jnp.mean(xv * xv, axis=-1, keepdims=True) + EPS)
        h = ((xv * r) * g_ref[...]).astype(BF16)
        h_ref[...] = h
        row0 = 0
        for ref in (qkv_ref, uy_ref, gg_ref):
            for c0 in range(0, ref.shape[1], TILE):
                arrived(row0, TILE)
                z = _dot_nt(h, w[row0:row0 + TILE, :]) + b_ref[:, row0:row0 + TILE]
                ref[:, c0:c0 + TILE] = z.astype(ref.dtype)
                row0 += TILE

    tok = lambda width: pl.BlockSpec((tm, width), lambda i: (i, 0))
    return pl.pallas_call(
        body, name="in_proj",
        out_shape=(jax.ShapeDtypeStruct((T, 3 * D_ATT), BF16),
                   jax.ShapeDtypeStruct((T, 2 * D_REC), F32),
                   jax.ShapeDtypeStruct((T, 2 * D), F32),
                   jax.ShapeDtypeStruct((T, D), BF16)),
        grid=(T // tm,),
        in_specs=[tok(D), pl.BlockSpec((1, D), lambda i: (0, 0)), pl.BlockSpec(memory_space=pl.ANY),
                  pl.BlockSpec((1, D_IN), lambda i: (0, 0))],
        out_specs=(tok(3 * D_ATT), tok(2 * D_REC), tok(2 * D), tok(D)),
        scratch_shapes=[pltpu.VMEM((D_IN, D), BF16), pltpu.SemaphoreType.DMA((N_DZ_TILES,))],
        compiler_params=_params(dimension_semantics=("arbitrary",)),
    )(x, g1, w_in_t, b_in)


def _dz_specs(rows, tile_of, row_of):
    def spec(off, n, per_plane):
        def index(*ids):
            t = jnp.clip(tile_of(*ids) - off, 0, n - 1)
            return (t // per_plane, row_of(*ids), t % per_plane)
        return pl.BlockSpec((1, rows, TILE), index)
    return [spec(off, n, per) for off, n, per in DZ_ARRAYS]


def _dh_norm1_bwd(dz, w_in_t, x, g1, dx1):
    tm = 512

    def body(dqkv_ref, duy_ref, dgg_ref, w_hbm, x_ref, g_ref, dx1_ref, gx_ref, dg_ref, w, sems):
        first_step = pl.program_id(0) == 0
        arrived = _stream_in(w_hbm, w, sems, TILE, first_step)

        @pl.when(first_step)
        def _():
            dg_ref[...] = jnp.zeros_like(dg_ref)

        dh, row0 = None, 0
        for ref in (dqkv_ref, duy_ref, dgg_ref):
            for plane in range(ref.shape[0]):
                cols = ref.shape[2]
                arrived(row0, cols)
                part = _dot(ref[plane], w[row0:row0 + cols, :])
                dh = part if dh is None else dh + part
                row0 += cols
        xv = x_ref[...]
        r = lax.rsqrt(jnp.mean(xv * xv, axis=-1, keepdims=True) + EPS)
        xh = xv * r
        dg_ref[...] += jnp.sum(dh * xh, axis=0, keepdims=True)
        gx_ref[...] = dx1_ref[...] + _rms_bwd(dh, xh, r, g_ref[...])

    tok = pl.BlockSpec((tm, D), lambda i: (i, 0))
    vec = pl.BlockSpec((1, D), lambda i: (0, 0))
    planes = lambda a: pl.BlockSpec((a.shape[0], tm, a.shape[2]), lambda i: (0, i, 0))
    return pl.pallas_call(
        body, name="dh_norm1_bwd",
        out_shape=(jax.ShapeDtypeStruct((T, D), F32), jax.ShapeDtypeStruct((1, D), F32)),
        grid=(T // tm,),
        in_specs=[planes(a) for a in dz] + [pl.BlockSpec(memory_space=pl.ANY), tok, vec, tok],
        out_specs=(tok, vec),
        scratch_shapes=[pltpu.VMEM((D_IN, D), BF16), pltpu.SemaphoreType.DMA((N_DZ_TILES,))],
        compiler_params=_params(dimension_semantics=("arbitrary",)),
    )(*dz, w_in_t, x, g1, dx1)


def _grad_w_in(dz, h):
    def body(*refs):
        seg_refs = refs[:3]
        h_ref, gw_ref, gb_ref = refs[3:]
        j = pl.program_id(0)

        for s, (off, n, _) in enumerate(DZ_ARRAYS):
            @pl.when((j >= off) & (j < off + n))
            def _(s=s):
                a = seg_refs[s][0]
                gw_ref[...] = _dot_tn(a, h_ref[...]).astype(BF16)
                gb_ref[...] = jnp.sum(a.astype(F32), axis=0, keepdims=True)

    return pl.pallas_call(
        body, name="grad_w_in",
        out_shape=(jax.ShapeDtypeStruct((D_IN, D), BF16), jax.ShapeDtypeStruct((1, D_IN), F32)),
        grid=(N_DZ_TILES,),
        in_specs=_dz_specs(T, lambda j: j, lambda j: 0) + [pl.BlockSpec((T, D), lambda j: (0, 0))],
        out_specs=(pl.BlockSpec((TILE, D), lambda j: (j, 0)), pl.BlockSpec((1, TILE), lambda j: (0, j))),
        compiler_params=_params(dimension_semantics=("parallel",)),
    )(*dz, h)


def _rpb_rows(rpb):
    padded = jnp.pad(rpb, ((0, 0), (0, 0), (0, GRID_W - N_RPB_C)))
    rows = [padded[:, WIN_H - 1 - oi: 2 * WIN_H - 1 - oi].reshape(N_HEADS // HG, HG, KWIN)
            for oi in range(WIN_H)]
    return jnp.stack(rows, axis=0)


SKEW = KWIN - (WIN_W - 1)


MASKED = -1e30


def _bias_tiles(rows_ref, valid, bias_s):
    for oi in range(WIN_H):
        for hh in range(HG):
            row = jnp.broadcast_to(rows_ref[oi, 0, hh:hh + 1, :], (GRID_W, KWIN))
            tile = pltpu.roll(row, SKEW, 1, stride=1, stride_axis=0)
            bias_s[oi, hh * GRID_W:(hh + 1) * GRID_W, :] = jnp.where(valid, tile, MASKED)


def _bias_tile_grads(gb_s, flip, out_ref):
    for oi in range(WIN_H):
        for hh in range(HG):
            g = _dot_exact(flip, gb_s[oi, hh * GRID_W:(hh + 1) * GRID_W, :])
            back = pltpu.roll(g, KWIN - (GRID_W - WIN_W), 1, stride=1, stride_axis=0)
            out_ref[0, oi, hh:hh + 1, :] = jnp.sum(back, axis=0, keepdims=True)


def _rpb_fold(row_grads):
    g = row_grads.transpose(1, 0, 2, 3).reshape(WIN_H, N_HEADS, WIN_H, GRID_W)
    g = g.transpose(0, 2, 1, 3)

    def body(g_ref, o_ref):
        for dr in range(N_RPB_R):
            terms = [g_ref[oi, i] for oi in range(WIN_H) for i in range(WIN_H) if i - oi + WIN_H - 1 == dr]
            acc = terms[0]
            for term in terms[1:]:
                acc = acc + term
            o_ref[dr] = acc

    out = pl.pallas_call(
        body, name="rpb_fold",
        out_shape=jax.ShapeDtypeStruct((N_RPB_R, N_HEADS, GRID_W), F32),
    )(g)
    return out.transpose(1, 0, 2)[:, :, :N_RPB_C]


ATT_GROUPS = N_HEADS // HG
ATT_UNROLL = 2


def _stacked(rows64, same_head):
    return jnp.where(same_head, jnp.concatenate([rows64] * HG, axis=0), jnp.zeros((), BF16))


def _own_heads(stacked):
    head = lax.broadcasted_iota(jnp.int32, (GRID_W, HC), 1) // DH
    out = stacked[:GRID_W]
    for h in range(1, HG):
        out = jnp.where(head == h, stacked[h * GRID_W:(h + 1) * GRID_W], out)
    return out


def _att_scores(q_ref, k_ref, bias_ref, same_head, r):
    rs = jnp.clip(r - WIN_H // 2, 0, ROWS - WIN_H)
    oi = r - rs
    q0 = pl.multiple_of(r * GRID_W, GRID_W)
    k0 = pl.multiple_of(rs * GRID_W, GRID_W)
    q2 = _stacked(q_ref[pl.ds(q0, GRID_W), :] * (DH ** -0.5), same_head)
    kw = k_ref[pl.ds(k0, KWIN), :]
    s = _dot_nt(q2, kw) + bias_ref[oi]
    e = jnp.exp(s - jnp.max(s, axis=-1, keepdims=True))
    return e, 1.0 / jnp.sum(e, axis=-1, keepdims=True), q2, kw, q0, k0, oi


def _att_specs():
    col = lambda off: pl.BlockSpec((T, HC), lambda g: (0, g + off * ATT_GROUPS))
    tables = [pl.BlockSpec((WIN_H, 1, HG, KWIN), lambda g: (0, g, 0, 0)),
              pl.BlockSpec((GRID_W, KWIN), lambda g: (0, 0)),
              pl.BlockSpec((HQ, HC), lambda g: (0, 0))]
    return col, tables, pltpu.VMEM((WIN_H, HQ, KWIN), F32)


def _att_fwd(qkv, bias_rows):
    valid_np, same_head_np = _att_tables()

    def body(q_ref, k_ref, v_ref, rows_ref, valid_ref, head_ref, o_ref, bias_s):
        same_head = head_ref[...] > 0.5
        _bias_tiles(rows_ref, valid_ref[...] > 0.5, bias_s)

        def row(r, carry):
            e, rl, _, _, q0, k0, _ = _att_scores(q_ref, k_ref, bias_s, same_head, r)
            o2 = _dot((e * rl).astype(BF16), v_ref[pl.ds(k0, KWIN), :])
            o_ref[pl.ds(q0, GRID_W), :] = _own_heads(o2).astype(BF16)
            return carry

        lax.fori_loop(0, ROWS, row, 0, unroll=ATT_UNROLL)

    col, tables, tiles = _att_specs()
    return pl.pallas_call(
        body, name="att_fwd",
        out_shape=jax.ShapeDtypeStruct((T, D_ATT), BF16),
        grid=(ATT_GROUPS,),
        in_specs=[col(0), col(1), col(2)] + tables,
        out_specs=col(0),
        scratch_shapes=[tiles],
        compiler_params=_params(dimension_semantics=("parallel",)),
    )(qkv, qkv, qkv, bias_rows, jnp.asarray(valid_np), jnp.asarray(same_head_np))


def _att_bwd(qkv, bias_rows, datt, after):
    valid_np, same_head_np = _att_tables()

    def body(q_ref, k_ref, v_ref, do_ref, rows_ref, valid_ref, head_ref, flip_ref,
             dqkv_ref, grows_ref, dk_acc, dv_acc, bias_s, gb_s):
        same_head = head_ref[...] > 0.5
        dk_acc[...] = jnp.zeros_like(dk_acc)
        dv_acc[...] = jnp.zeros_like(dv_acc)
        gb_s[...] = jnp.zeros_like(gb_s)
        _bias_tiles(rows_ref, valid_ref[...] > 0.5, bias_s)

        def row(r, carry):
            e, rl, q2, kw, q0, k0, oi = _att_scores(q_ref, k_ref, bias_s, same_head, r)
            do2 = _stacked(do_ref[pl.ds(q0, GRID_W), :], same_head)
            vw = v_ref[pl.ds(k0, KWIN), :]
            p = e * rl
            dp = _dot_nt(do2, vw)
            ds = p * (dp - jnp.sum(dp * p, axis=-1, keepdims=True))
            p16 = p.astype(BF16)
            ds16 = ds.astype(BF16)
            dv_acc[pl.ds(k0, KWIN), :] += _dot_tn(p16, do2)
            dk_acc[pl.ds(k0, KWIN), :] += _dot_tn(ds16, q2)
            dq2 = _dot(ds16, kw) * (DH ** -0.5)
            dqkv_ref[0, pl.ds(q0, GRID_W), :] = _own_heads(dq2).astype(BF16)
            gb_s[oi] += ds
            return carry

        lax.fori_loop(0, ROWS, row, 0, unroll=ATT_UNROLL)
        dqkv_ref[1] = dk_acc[...].astype(BF16)
        dqkv_ref[2] = dv_acc[...].astype(BF16)
        _bias_tile_grads(gb_s, flip_ref[...], grows_ref)

    col, tables, tiles = _att_specs()
    return pl.pallas_call(
        body, name="att_bwd",
        out_shape=(jax.ShapeDtypeStruct((3, T, D_ATT), BF16),
                   jax.ShapeDtypeStruct((ATT_GROUPS, WIN_H, HG, KWIN), F32)),
        grid=(ATT_GROUPS,),
        in_specs=[col(0), col(1), col(2), col(0)] + tables + [pl.BlockSpec((GRID_W, GRID_W), lambda g: (0, 0))],
        out_specs=(pl.BlockSpec((3, T, HC), lambda g: (0, 0, g)),
                   pl.BlockSpec((1, WIN_H, HG, KWIN), lambda g: (g, 0, 0, 0))),
        scratch_shapes=[pltpu.VMEM((T, HC), F32), pltpu.VMEM((T, HC), F32), tiles, tiles],
        compiler_params=_params(dimension_semantics=("parallel",)),
    )(qkv, qkv, qkv, datt, bias_rows, jnp.asarray(valid_np) + after, jnp.asarray(same_head_np),
      jnp.asarray(np.eye(GRID_W, dtype=np.float32)[::-1].copy()))


def _conv_taps(up):
    return (_shift_rows(up, 2), _shift_rows(up, 1), up, _shift_rows(up, -1))


def _pair_block_diag(w_pair, dup, same_half):
    return jnp.where(same_half, _dot(w_pair.astype(BF16), dup), 0.0).astype(BF16)


def _gates(u, u16, wa, ba, wi, bi, lam):
    r = _sigmoid(_dot(u16, wa) + ba)
    ig = _sigmoid(_dot(u16, wi) + bi)
    sp = _softplus(-lam)
    log_a = (-LRU_C) * r * sp
    a = jnp.exp(log_a)
    mult2 = jnp.maximum(_one_minus_square(log_a, a), 0.0)
    return r, ig, sp, a, jnp.sqrt(mult2), mult2


SCAN_BLOCKS = 2


def _scans(jobs):
    c = jobs[0][0].shape[1]
    nblk = T // 8
    rows = lax.broadcasted_iota(jnp.int32, (8, c), 0)

    def block(a, b, reverse):
        for s in (1, 2, 4):
            if reverse:
                keep = rows < 8 - s
                a_s = jnp.where(keep, pltpu.roll(a, 8 - s, 0), 1.0)
                b_s = jnp.where(keep, pltpu.roll(b, 8 - s, 0), 0.0)
            else:
                keep = rows >= s
                a_s = jnp.where(keep, pltpu.roll(a, s, 0), 1.0)
                b_s = jnp.where(keep, pltpu.roll(b, s, 0), 0.0)
            b = a * b_s + b
            a = a * a_s
        return a, b

    def step(i, carry):
        out = []
        for (a_ref, b_ref, h_ref, reverse), h_prev in zip(jobs, carry):
            for u in range(SCAN_BLOCKS):
                blk = i * SCAN_BLOCKS + u
                if reverse:
                    blk = nblk - 1 - blk
                t0 = pl.multiple_of(blk * 8, 8)
                a, b = block(a_ref[pl.ds(t0, 8), :], b_ref[pl.ds(t0, 8), :], reverse)
                h = a * h_prev + b
                h_ref[pl.ds(t0, 8), :] = h
                h_prev = jnp.broadcast_to(h[0:1] if reverse else h[7:8], (8, c))
            out.append(h_prev)
        return tuple(out)

    lax.fori_loop(0, nblk // SCAN_BLOCKS, step, tuple(jnp.zeros((8, c), F32) for _ in jobs))


def _rec_specs():
    tok = lambda off: pl.BlockSpec((T, CG), lambda g: (0, g + off))
    per_ch = lambda rows: pl.BlockSpec((rows, CG), lambda g: (0, g))
    wspec = pl.BlockSpec((2, 1, CG, REC_BLOCK), lambda g: (0, g, 0, 0))
    const = lambda shape: pl.BlockSpec(shape, lambda g: (0, 0))
    return tok, per_ch, wspec, const


def _rec_fwd(uy, conv_w, conv_b, w_a, b_a, w_i, b_i, lam):
    tok, per_ch, wspec, const = _rec_specs()

    def body(up_ref, yb_ref, cw_ref, cb_ref, wa_ref, ba_ref, wi_ref, bi_ref, lam_ref, dup_ref, half_ref,
             hf_ref, hb_ref, yrec_ref, am_ref, bx_f, bx_b):
        dup = dup_ref[...]
        same_half = half_ref[...] > 0.5
        taps = _conv_taps(up_ref[...])
        u = cb_ref[...]
        for j in range(4):
            u = u + taps[j] * cw_ref[j:j + 1, :]
        u16 = u.astype(BF16)
        for d, bx_s in enumerate((bx_f, bx_b)):
            wa = _pair_block_diag(wa_ref[d, 0], dup, same_half)
            wi = _pair_block_diag(wi_ref[d, 0], dup, same_half)
            _, ig, _, a, mult, _ = _gates(u, u16, wa, ba_ref[d:d + 1, :], wi, bi_ref[d:d + 1, :],
                                       lam_ref[d:d + 1, :])
            am_ref[2 * d] = a
            am_ref[2 * d + 1] = mult
            bx_s[...] = mult * (ig * u)
        _scans([(am_ref.at[0], bx_f, hf_ref, False), (am_ref.at[2], bx_b, hb_ref, True)])
        gelu, _ = _gelu_and_grad(yb_ref[...])
        yrec_ref[...] = ((hf_ref[...] + hb_ref[...]) * gelu).astype(BF16)

    return pl.pallas_call(
        body, name="rec_fwd",
        out_shape=(jax.ShapeDtypeStruct((T, D_REC), F32), jax.ShapeDtypeStruct((T, D_REC), F32),
                   jax.ShapeDtypeStruct((T, D_REC), BF16), jax.ShapeDtypeStruct((4, T, D_REC), F32)),
        grid=(N_CG,),
        in_specs=[tok(0), tok(N_CG), per_ch(4), per_ch(1), wspec, per_ch(2), wspec, per_ch(2), per_ch(2),
                  const((REC_BLOCK, CG)), const((CG, CG))],
        out_specs=(tok(0), tok(0), tok(0), pl.BlockSpec((4, T, CG), lambda g: (0, 0, g))),
        scratch_shapes=[pltpu.VMEM((T, CG), F32)] * 2,
        compiler_params=_params(dimension_semantics=("parallel",)),
    )(uy, uy, conv_w, conv_b, w_a, b_a, w_i, b_i, lam,
      jnp.asarray(_dup_table(), BF16), jnp.asarray(_pair_mask()))


def _rec_bwd(uy, hf, hb, am, dyrec, conv_w, conv_b, w_a, b_a, w_i, b_i, lam):
    tok, per_ch, wspec, const = _rec_specs()

    def body(up_ref, yb_ref, hf_ref, hb_ref, am_ref, dy_ref, cw_ref, cb_ref, wa_ref, ba_ref, wi_ref, bi_ref,
             lam_ref, dup_ref, dupt_ref, half_ref,
             duy_ref, dcw_ref, dcb_ref, dwa_ref, dba_ref, dwi_ref, dbi_ref, dlam_ref,
             a_s0, a_s1, dh_s, g_s0, g_s1):
        dup = dup_ref[...]
        dup_t = dupt_ref[...]
        same_half = half_ref[...] > 0.5
        taps = _conv_taps(up_ref[...])
        u = cb_ref[...]
        for j in range(4):
            u = u + taps[j] * cw_ref[j:j + 1, :]
        u16 = u.astype(BF16)
        gelu, dgelu = _gelu_and_grad(yb_ref[...])
        dy = dy_ref[...]
        duy_ref[1] = (dy * (hf_ref[...] + hb_ref[...]) * dgelu).astype(BF16)
        dh_s[...] = dy * gelu
        a_s0[...] = _shift_rows(am_ref[0], -1)
        a_s1[...] = _shift_rows(am_ref[2], 1)
        _scans([(a_s0, dh_s, g_s0, True), (a_s1, dh_s, g_s1, False)])
        du = jnp.zeros((T, CG), F32)
        for d, g_s in enumerate((g_s0, g_s1)):
            reverse = d == 1
            wa = _pair_block_diag(wa_ref[d, 0], dup, same_half)
            wi = _pair_block_diag(wi_ref[d, 0], dup, same_half)
            lam_d = lam_ref[d:d + 1, :]
            r = _sigmoid(_dot(u16, wa) + ba_ref[d:d + 1, :])
            ig = _sigmoid(_dot(u16, wi) + bi_ref[d:d + 1, :])
            sp = _softplus(-lam_d)
            a, mult = am_ref[2 * d], am_ref[2 * d + 1]
            mult2 = mult * mult
            g = g_s[...]
            h_prev = _shift_rows(hb_ref[...], -1) if reverse else _shift_rows(hf_ref[...], 1)
            da = g * h_prev
            dmult = g * (ig * u)
            dig = g * mult * u
            du = du + g * mult * ig
            dmult_dlog = jnp.where(mult2 > 0.0, -(a * a) * lax.rsqrt(mult2), 0.0)
            dlog_a = da * a + dmult * dmult_dlog
            dr = dlog_a * ((-LRU_C) * sp)
            dsp = jnp.sum(dlog_a * ((-LRU_C) * r), axis=0, keepdims=True)
            dlam_ref[d:d + 1, :] = dsp * (-_sigmoid(-lam_d))
            dga = dr * r * (1.0 - r)
            dgi = dig * ig * (1.0 - ig)
            dga16 = dga.astype(BF16)
            dgi16 = dgi.astype(BF16)
            du = du + _dot_nt(dga16, wa) + _dot_nt(dgi16, wi)
            dwa_ref[d, 0] = _dot_exact(jnp.where(same_half, _dot_tn(u16, dga16), 0.0), dup_t)
            dwi_ref[d, 0] = _dot_exact(jnp.where(same_half, _dot_tn(u16, dgi16), 0.0), dup_t)
            dba_ref[d:d + 1, :] = jnp.sum(dga, axis=0, keepdims=True)
            dbi_ref[d:d + 1, :] = jnp.sum(dgi, axis=0, keepdims=True)
        dcb_ref[...] = jnp.sum(du, axis=0, keepdims=True)
        for j in range(4):
            dcw_ref[j:j + 1, :] = jnp.sum(du * taps[j], axis=0, keepdims=True)
        dup_in = (_shift_rows(du, -2) * cw_ref[0:1, :] + _shift_rows(du, -1) * cw_ref[1:2, :]
                  + du * cw_ref[2:3, :] + _shift_rows(du, 1) * cw_ref[3:4, :])
        duy_ref[0] = dup_in.astype(BF16)

    wshape = jax.ShapeDtypeStruct((2, N_CG, CG, REC_BLOCK), F32)
    vec = lambda rows: jax.ShapeDtypeStruct((rows, D_REC), F32)
    dup_np = _dup_table()
    return pl.pallas_call(
        body, name="rec_bwd",
        out_shape=(jax.ShapeDtypeStruct((2, T, D_REC), BF16),
                   vec(4), vec(1), wshape, vec(2), wshape, vec(2), vec(2)),
        grid=(N_CG,),
        in_specs=[tok(0), tok(N_CG), tok(0), tok(0), pl.BlockSpec((4, T, CG), lambda g: (0, 0, g)), tok(0),
                  per_ch(4), per_ch(1), wspec, per_ch(2), wspec, per_ch(2), per_ch(2),
                  const((REC_BLOCK, CG)), const((CG, REC_BLOCK)), const((CG, CG))],
        out_specs=(pl.BlockSpec((2, T, CG), lambda g: (0, 0, g)),
                   per_ch(4), per_ch(1), wspec, per_ch(2), wspec, per_ch(2), per_ch(2)),
        scratch_shapes=[pltpu.VMEM((T, CG), F32)] * 5,
        compiler_params=_params(dimension_semantics=("parallel",)),
    )(uy, uy, hf, hb, am, dyrec, conv_w, conv_b, w_a, b_a, w_i, b_i, lam,
      jnp.asarray(dup_np, BF16), jnp.asarray(dup_np.T.copy()), jnp.asarray(_pair_mask()))


TM_MIX = 256


def _mix_specs():
    tok = lambda width, blk=0: pl.BlockSpec((TM_MIX, width), lambda i: (i, blk))
    full = lambda shape: pl.BlockSpec(shape, lambda i: (0, 0))
    return tok, full


def _mix_fwd(x, att, yrec, gg, w_att_o_t, w_rec_o, w_out):
    tok, full = _mix_specs()

    def body(x_ref, att_ref, yr_ref, ga_ref, gr_ref, wao_ref, wro_ref, wo_ref, x1_ref, mixed_ref):
        y_att = _dot_nt(att_ref[...], wao_ref[...])
        y_rec = _dot(yr_ref[...], wro_ref[...])
        mixed = (_sigmoid(ga_ref[...]) * y_att + _sigmoid(gr_ref[...]) * y_rec).astype(BF16)
        mixed_ref[...] = mixed
        x1_ref[...] = x_ref[...] + _dot(mixed, wo_ref[...])

    return pl.pallas_call(
        body, name="mix_fwd",
        out_shape=(jax.ShapeDtypeStruct((T, D), F32), jax.ShapeDtypeStruct((T, D), BF16)),
        grid=(T // TM_MIX,),
        in_specs=[tok(D), tok(D_ATT), tok(D_REC), tok(D, 0), tok(D, 1),
                  full((D, D_ATT)), full((D_REC, D)), full((D, D))],
        out_specs=(tok(D), tok(D)),
        compiler_params=_params(dimension_semantics=("parallel",)),
    )(x, att, yrec, gg, gg, w_att_o_t, w_rec_o, w_out)


def _mix_bwd(dx1, att, yrec, gg, w_att_o_t, w_rec_o, w_out):
    tok, full = _mix_specs()

    def body(dx_ref, att_ref, yr_ref, ga_ref, gr_ref, wao_ref, wro_ref, wo_ref,
             dgg_ref, dya_ref, dyr_ref, datt_ref, dyrp_ref):
        dmixed = _dot_nt(dx_ref[...].astype(BF16), wo_ref[...])
        y_att = _dot_nt(att_ref[...], wao_ref[...])
        y_rec = _dot(yr_ref[...], wro_ref[...])
        sa = _sigmoid(ga_ref[...])
        sr = _sigmoid(gr_ref[...])
        dgg_ref[0] = (dmixed * y_att * sa * (1.0 - sa)).astype(BF16)
        dgg_ref[1] = (dmixed * y_rec * sr * (1.0 - sr)).astype(BF16)
        dya = (dmixed * sa).astype(BF16)
        dyr = (dmixed * sr).astype(BF16)
        dya_ref[...] = dya
        dyr_ref[...] = dyr
        datt_ref[...] = _dot(dya, wao_ref[...]).astype(BF16)
        dyrp_ref[...] = _dot_nt(dyr, wro_ref[...])

    return pl.pallas_call(
        body, name="mix_bwd",
        out_shape=(jax.ShapeDtypeStruct((2, T, D), BF16),
                   jax.ShapeDtypeStruct((T, D), BF16), jax.ShapeDtypeStruct((T, D), BF16),
                   jax.ShapeDtypeStruct((T, D_ATT), BF16), jax.ShapeDtypeStruct((T, D_REC), F32)),
        grid=(T // TM_MIX,),
        in_specs=[tok(D), tok(D_ATT), tok(D_REC), tok(D, 0), tok(D, 1),
                  full((D, D_ATT)), full((D_REC, D)), full((D, D))],
        out_specs=(pl.BlockSpec((2, TM_MIX, D), lambda i: (0, i, 0)),
                   tok(D), tok(D), tok(D_ATT), tok(D_REC)),
        compiler_params=_params(dimension_semantics=("parallel",)),
    )(dx1, att, yrec, gg, gg, w_att_o_t, w_rec_o, w_out)


TM_FFN = 256
FF_CHUNK = 1024


def _ffn_loss(x1, target, g2, gf, w_ff1_t, w_ff2):
    n_chunks = D_FF // FF_CHUNK

    def body(x1_ref, tg_ref, g2_ref, gf_ref, w1_hbm, w2_hbm,
             loss_ref, dx1_ref, h2_ref, act_ref, dpre_ref, dx2_ref, dg2_ref, dgf_ref,
             w1, w2, relu_s, sems1, sems2):
        first_step = pl.program_id(0) == 0
        arrived1 = _stream_in(w1_hbm, w1, sems1, FF_CHUNK, first_step)
        arrived2 = _stream_in(w2_hbm, w2, sems2, FF_CHUNK, first_step)

        @pl.when(first_step)
        def _():
            loss_ref[...] = jnp.zeros_like(loss_ref)
            dg2_ref[...] = jnp.zeros_like(dg2_ref)
            dgf_ref[...] = jnp.zeros_like(dgf_ref)

        x1v = x1_ref[...]
        r2 = lax.rsqrt(jnp.mean(x1v * x1v, axis=-1, keepdims=True) + EPS)
        xh2 = x1v * r2
        h2 = (xh2 * g2_ref[...]).astype(BF16)
        h2_ref[...] = h2
        x2 = x1v
        for c in range(n_chunks):
            ff = slice(c * FF_CHUNK, (c + 1) * FF_CHUNK)
            arrived1(c * FF_CHUNK, FF_CHUNK)
            rl = jnp.maximum(_dot_nt(h2, w1[ff, :]), 0.0)
            relu_s[:, ff] = rl
            act = (rl * rl).astype(BF16)
            act_ref[:, ff] = act
            arrived2(c * FF_CHUNK, FF_CHUNK)
            x2 = x2 + _dot(act, w2[ff, :])
        r3 = lax.rsqrt(jnp.mean(x2 * x2, axis=-1, keepdims=True) + EPS)
        xh3 = x2 * r3
        err = xh3 * gf_ref[...] - tg_ref[...]
        loss_ref[...] += 0.5 * jnp.sum(jnp.mean(err * err, axis=-1, keepdims=True))
        dy = err * (1.0 / D)
        dgf_ref[...] += jnp.sum(dy * xh3, axis=0, keepdims=True)
        dx2 = _rms_bwd(dy, xh3, r3, gf_ref[...])
        dx2_16 = dx2.astype(BF16)
        dx2_ref[...] = dx2_16
        dh2 = jnp.zeros((TM_FFN, D), F32)
        for c in range(n_chunks):
            ff = slice(c * FF_CHUNK, (c + 1) * FF_CHUNK)
            dpre = (_dot_nt(dx2_16, w2[ff, :]) * (2.0 * relu_s[:, ff])).astype(BF16)
            dpre_ref[:, ff] = dpre
            dh2 = dh2 + _dot(dpre, w1[ff, :])
        dg2_ref[...] += jnp.sum(dh2 * xh2, axis=0, keepdims=True)
        dx1_ref[...] = dx2 + _rms_bwd(dh2, xh2, r2, g2_ref[...])

    tok = lambda width: pl.BlockSpec((TM_FFN, width), lambda i: (i, 0))
    vec = pl.BlockSpec((1, D), lambda i: (0, 0))
    hbm = pl.BlockSpec(memory_space=pl.ANY)
    return pl.pallas_call(
        body, name="ffn_loss",
        out_shape=(jax.ShapeDtypeStruct((8, 128), F32), jax.ShapeDtypeStruct((T, D), F32),
                   jax.ShapeDtypeStruct((T, D), BF16), jax.ShapeDtypeStruct((T, D_FF), BF16),
                   jax.ShapeDtypeStruct((T, D_FF), BF16), jax.ShapeDtypeStruct((T, D), BF16),
                   jax.ShapeDtypeStruct((1, D), F32), jax.ShapeDtypeStruct((1, D), F32)),
        grid=(T // TM_FFN,),
        in_specs=[tok(D), tok(D), vec, vec, hbm, hbm],
        out_specs=(pl.BlockSpec((8, 128), lambda i: (0, 0)), tok(D), tok(D), tok(D_FF), tok(D_FF), tok(D),
                   vec, vec),
        scratch_shapes=[pltpu.VMEM((D_FF, D), BF16), pltpu.VMEM((D_FF, D), BF16),
                        pltpu.VMEM((TM_FFN, D_FF), F32),
                        pltpu.SemaphoreType.DMA((D_FF // FF_CHUNK,)), pltpu.SemaphoreType.DMA((D_FF // FF_CHUNK,))],
        compiler_params=_params(dimension_semantics=("arbitrary",)),
    )(x1, target, g2, gf, w_ff1_t, w_ff2)


def _local_step(x, target, p, late_weights, reduce_early):
    bias = _rpb_rows(p["rpb"])
    pairs = lambda w: w.reshape(2, N_CG, CG, REC_BLOCK)
    w_a, w_i = pairs(p["w_rg_a"]), pairs(p["w_rg_i"])
    rec_params = (p["conv_w"], p["conv_b"], w_a, p["b_rg_a"], w_i, p["b_rg_i"], p["lru_lambda"])

    qkv, uy, gg, h = _in_proj(x, p["ln1_g"], p["w_in_t"], p["b_in"])
    att = _att_fwd(qkv, bias)
    hf, hb, yrec, am = _rec_fwd(uy, *rec_params)
    p = {**p, **late_weights(yrec, 0)}
    x1, mixed = _mix_fwd(x, att, yrec, gg, p["w_att_o_t"], p["w_rec_o"], p["w_out"])
    p = {**p, **late_weights(x1, 1)}
    loss8, dx1, h2, act, dpre, dx2, g_ln2, g_lnf = _ffn_loss(
        x1, target, p["ln2_g"], p["lnf_g"], p["w_ff1_t"], p["w_ff2"])

    dgg, dya, dyr, datt, dyrp = _mix_bwd(dx1, att, yrec, gg, p["w_att_o_t"], p["w_rec_o"], p["w_out"])
    duy, g_cw, g_cb, g_wa, g_ba, g_wi, g_bi, g_lam = _rec_bwd(uy, hf, hb, am, dyrp, *rec_params)
    blocks = lambda g: g.reshape(2, N_REC_BLOCKS, REC_BLOCK, REC_BLOCK)
    grads = {
        "w_att_o_t": _matmul(dya, att, "tn", BF16, "g_w_att_o"),
        "conv_w": g_cw, "conv_b": g_cb, "w_rg_a": blocks(g_wa), "b_rg_a": g_ba,
        "w_rg_i": blocks(g_wi), "b_rg_i": g_bi, "lru_lambda": g_lam,
        "w_rec_o": _matmul(yrec, dyr, "tn", BF16, "g_w_rec_o"),
        "w_out": _matmul(mixed, dx1, "tn", BF16, "g_w_out"),
        "ln2_g": g_ln2,
        "w_ff1_t": _matmul(dpre, h2, "tn", BF16, "g_w_ff1"),
        "w_ff2": _matmul(act, dx2, "tn", BF16, "g_w_ff2"),
        "lnf_g": g_lnf,
    }
    dqkv, gbias = _att_bwd(qkv, bias, datt, reduce_early(grads))
    dz = (dqkv, duy, dgg)
    grad_x, g_ln1 = _dh_norm1_bwd(dz, p["w_in_t"], x, p["ln1_g"], dx1)
    g_w_in_t, g_b_in = _grad_w_in(dz, h)
    grads.update(ln1_g=g_ln1, w_in_t=g_w_in_t, b_in=g_b_in, rpb=_rpb_fold(gbias))
    return loss8[0:1, 0:1], grad_x, grads


MESH_ID = pl.DeviceIdType.MESH
ANY = pl.BlockSpec(memory_space=pl.ANY)

CHAN_BLOCK_ROWS = 32
GATE_ROWS = 2 * 2 * N_REC_BLOCKS * REC_BLOCK * REC_BLOCK // (N_DEV * D)
SECTIONS = (("w_in_t", 704, D), ("w_rec_o", 128, D), ("w_out", 128, D), ("w_ff1_t", 512, D),
            ("w_ff2", 512, D), ("chan", CHAN_BLOCK_ROWS, D), ("w_att_o_t", 128, D_ATT),
            ("gates", GATE_ROWS, D))
N_SEC = len(SECTIONS)
N_CHAN_ROWS = 10
CHAN = (("conv_w", 4), ("b_rg_a", 2), ("b_rg_i", 2), ("lru_lambda", 2))


def _position():
    return lax.axis_index("x"), lax.axis_index("y"), lax.axis_index("c")


def _other_chips(x, y):
    return [(1 - x, y), (x, 1 - y), (1 - x, 1 - y)]


PASS_ON_IDS, PAIR_EARLY_ID, PAIR_LATE_ID = (1, 4), 2, 3


def _pair_handshake(x, y, c):
    barrier = pltpu.get_barrier_semaphore()
    pl.semaphore_signal(barrier, inc=1, device_id=(x, y, 1 - c), device_id_type=MESH_ID)
    pl.semaphore_wait(barrier, 1)


def _block_of(ref, dev, rows):
    return ref.at[pl.ds(pl.multiple_of(dev * rows, 16), rows)]


def _all_gather(shards, name):
    ns = len(shards)

    def body(*refs):
        x_refs, out_refs, done_ref = refs[:ns], refs[ns:2 * ns], refs[2 * ns]
        send_sems, recv_sems, local_sems = refs[2 * ns + 1:]
        done_ref[0, 0] = 0.0
        x, y, c = _position()
        me, sibling = (x, y, c), (x, y, 1 - c)
        x_nbr, y_nbr, diagonal = _other_chips(x, y)
        north = c == 1
        relay_from = (jnp.where(north, x_nbr[0], y_nbr[0]), jnp.where(north, x_nbr[1], y_nbr[1]))
        relay_to = (jnp.where(north, y_nbr[0], x_nbr[0]), jnp.where(north, y_nbr[1], x_nbr[1]))

        def rows(s, px, py, pc):
            return _block_of(out_refs[s], 4 * px + 2 * py + pc, shards[s].shape[0])

        def copy(k, s, block, to, from_shard=False):
            return pltpu.make_async_remote_copy(
                src_ref=x_refs[s] if from_shard else rows(s, *block), dst_ref=rows(s, *block),
                send_sem=send_sems.at[k * ns + s], recv_sem=recv_sems.at[k * ns + s],
                device_id=to, device_id_type=MESH_ID)

        sections = range(ns)
        mine = [pltpu.make_async_copy(x_refs[s], rows(s, *me), local_sems.at[s]) for s in sections]
        sent = [copy(k, s, me, to, True) for k, to in enumerate((sibling, (*x_nbr, c), (*y_nbr, c)))
                for s in sections]
        for cp in mine + sent:
            cp.start()
        for s in sections:
            copy(1, s, (*x_nbr, c), me).wait_recv()
            copy(2, s, (*y_nbr, c), me).wait_recv()
            sent += [copy(3, s, (*relay_from, c), (*relay_to, c)),
                     copy(4, s, (*x_nbr, c), sibling), copy(5, s, (*y_nbr, c), sibling)]
            for cp in sent[-3:]:
                cp.start()
        for s in sections:
            copy(3, s, (*diagonal, c), me).wait_recv()
            sent.append(copy(6, s, (*diagonal, c), sibling))
            sent[-1].start()
        for s in sections:
            copy(0, s, sibling, me).wait_recv()
            for k, chip in ((4, x_nbr), (5, y_nbr), (6, diagonal)):
                copy(k, s, (*chip, 1 - c), me).wait_recv()
        for cp in sent:
            cp.wait_send()
        for cp in mine:
            cp.wait()

    return pl.pallas_call(
        body, name=name,
        out_shape=tuple(jax.ShapeDtypeStruct((N_DEV * s.shape[0], s.shape[1]), s.dtype) for s in shards)
        + (jax.ShapeDtypeStruct((1, 1), F32),),
        in_specs=[ANY] * ns,
        out_specs=(ANY,) * ns + (pl.BlockSpec(memory_space=pltpu.SMEM),),
        scratch_shapes=[pltpu.SemaphoreType.DMA((7 * ns,)), pltpu.SemaphoreType.DMA((7 * ns,)),
                        pltpu.SemaphoreType.DMA((ns,))],
    )(*shards)


HBM = pl.BlockSpec(memory_space=pltpu.HBM)
SEM = pl.BlockSpec(memory_space=pltpu.SEMAPHORE)
EFFECT = pltpu.SideEffectType.DATAFLOW_SIDE_EFFECTING


def _in_hbm(a):
    return pltpu.with_memory_space_constraint(a, pltpu.HBM)


def _first_hop_copies(shards, x_refs, zones, send_sems, recv_sems):
    ns = len(shards)
    x, y, c = _position()
    targets = [(x, y, 1 - c)] + [(cx, cy, c) for cx, cy in _other_chips(x, y)]
    return [pltpu.make_async_remote_copy(
        src_ref=x_refs[s], dst_ref=_block_of(zones[s], 4 * x + 2 * y + c, shards[s].shape[0]),
        send_sem=send_sems.at[k * ns + s], recv_sem=recv_sems.at[k * ns + s],
        device_id=to, device_id_type=MESH_ID)
        for k, to in enumerate(targets) for s in range(ns)]


def _after_all(arrays, name):
    def body(*refs):
        refs[-1][...] = jnp.zeros_like(refs[-1])

    return pl.pallas_call(
        body, name=name,
        out_shape=jax.ShapeDtypeStruct((8, LANES), F32),
        in_specs=[pl.BlockSpec(memory_space=pl.ANY)] * len(arrays),
        out_specs=pl.BlockSpec(memory_space=pltpu.VMEM),
    )(*arrays)


def _own_blocks_placed(shards, after):
    ns = len(shards)
    x, y, c = _position()
    me = jnp.reshape(4 * x + 2 * y + c, (1,)).astype(jnp.int32)
    shards = [*shards[:-1], shards[-1] + after.astype(shards[-1].dtype)]

    def body(me_ref, *refs):
        for s in range(ns):
            refs[ns + s][...] = refs[s][...]

    return pl.pallas_call(
        body, name="own_blocks_placed",
        out_shape=tuple(jax.ShapeDtypeStruct((N_DEV * s.shape[0], s.shape[1]), s.dtype) for s in shards),
        grid_spec=pltpu.PrefetchScalarGridSpec(
            num_scalar_prefetch=1, grid=(1,),
            in_specs=[pl.BlockSpec(s.shape, lambda i, me: (0, 0)) for s in shards],
            out_specs=tuple(pl.BlockSpec(s.shape, lambda i, me: (me[0], 0)) for s in shards)),
        compiler_params=_params(dimension_semantics=("arbitrary",)),
    )(me, *shards)


def _gather_start(shards, after, name):
    ns = len(shards)
    zones = _own_blocks_placed(shards, after)

    def body(*refs):
        for cp in _first_hop_copies(shards, refs[:ns], refs[ns:2 * ns], refs[2 * ns], refs[2 * ns + 1]):
            cp.start()
        refs[-1][...] = jnp.zeros_like(refs[-1])

    out = pl.pallas_call(
        body, name=name,
        out_shape=(pltpu.SemaphoreType.DMA((4 * ns,)), pltpu.SemaphoreType.DMA((4 * ns,)),
                   *[pltpu.HBM(a.shape, a.dtype) for a in (*shards, *zones)],
                   jax.ShapeDtypeStruct((8, LANES), F32)),
        in_specs=[HBM] * (2 * ns),
        out_specs=(SEM, SEM, *[HBM] * (2 * ns), pl.BlockSpec(memory_space=pltpu.VMEM)),
        input_output_aliases={i: 2 + i for i in range(2 * ns)},
        compiler_params=pltpu.CompilerParams(has_side_effects=EFFECT),
    )(*[_in_hbm(a) for a in shards], *[_in_hbm(a) for a in zones])
    return out[0], out[1], out[2:2 + ns], out[2 + ns:2 + 2 * ns], out[-1]


def _gather_wait(send_sems, recv_sems, shards, zones, which, after, name):
    ns = len(shards)

    def body(*refs):
        copies = _first_hop_copies(shards, refs[:ns], refs[ns:2 * ns], refs[2 * ns], refs[2 * ns + 1])
        for i, cp in enumerate(copies):
            if i % ns in which:
                cp.wait_send()
                cp.wait_recv()

    out = pl.pallas_call(
        body, name=name,
        out_shape=tuple(pltpu.HBM(a.shape, a.dtype) for a in (*shards, *zones)),
        in_specs=[HBM] * (2 * ns) + [SEM, SEM, ANY],
        out_specs=(HBM,) * (2 * ns),
        input_output_aliases={i: i for i in range(2 * ns)},
        compiler_params=pltpu.CompilerParams(has_side_effects=EFFECT),
    )(*shards, *zones, send_sems, recv_sems, after)
    return out[:ns], out[ns:]


def _gather_pass_on(rows, zones, barrier_id, name):
    ns = len(zones)

    def body(*refs):
        in_refs, out_refs = refs[:ns], refs[ns:2 * ns]
        send_sems, recv_sems = refs[2 * ns:]
        x, y, c = _position()
        _pair_handshake(x, y, c)
        copies = [pltpu.make_async_remote_copy(
            src_ref=_block_of(in_refs[s], 4 * cx + 2 * cy + c, rows[s]),
            dst_ref=_block_of(out_refs[s], 4 * cx + 2 * cy + c, rows[s]),
            send_sem=send_sems.at[j * ns + s], recv_sem=recv_sems.at[j * ns + s],
            device_id=(x, y, 1 - c), device_id_type=MESH_ID)
            for j, (cx, cy) in enumerate(_other_chips(x, y)) for s in range(ns)]
        for cp in copies:
            cp.start()
        for cp in copies:
            cp.wait_recv()
        for cp in copies:
            cp.wait_send()

    return pl.pallas_call(
        body, name=name,
        out_shape=tuple(jax.ShapeDtypeStruct(z.shape, z.dtype) for z in zones),
        in_specs=[ANY] * ns, out_specs=(ANY,) * ns,
        input_output_aliases={i: i for i in range(ns)},
        scratch_shapes=[pltpu.SemaphoreType.DMA((3 * ns,)), pltpu.SemaphoreType.DMA((3 * ns,))],
        compiler_params=pltpu.CompilerParams(collective_id=barrier_id),
    )(*zones)


def _pair_copies(sections, g_refs, land, send_sems, recv_sems):
    ns = len(sections)
    x, y, c = _position()
    return [pltpu.make_async_remote_copy(
        src_ref=_block_of(g_refs[s], 2 * k + 1 - c, rows), dst_ref=land[s].at[k],
        send_sem=send_sems.at[k * ns + s], recv_sem=recv_sems.at[k * ns + s],
        device_id=(x, y, 1 - c), device_id_type=MESH_ID)
        for k in range(N_CHIPS) for s, (_, rows, _) in enumerate(sections)]


def _pair_exchange_start(sections, grads, barrier_id, name):
    ns = len(sections)

    def body(*refs):
        _pair_handshake(*_position())
        for cp in _pair_copies(sections, refs[:ns], refs[ns:2 * ns], refs[2 * ns], refs[2 * ns + 1]):
            cp.start()
        refs[-1][...] = jnp.zeros_like(refs[-1])

    zones = [lax.empty((N_CHIPS, rows, cols), BF16) for _, rows, cols in sections]
    n = N_CHIPS * ns
    out = pl.pallas_call(
        body, name=name,
        out_shape=(pltpu.SemaphoreType.DMA((n,)), pltpu.SemaphoreType.DMA((n,)),
                   *[pltpu.HBM(a.shape, a.dtype) for a in (*grads, *zones)],
                   jax.ShapeDtypeStruct((8, LANES), F32)),
        in_specs=[HBM] * (2 * ns),
        out_specs=(SEM, SEM, *[HBM] * (2 * ns), pl.BlockSpec(memory_space=pltpu.VMEM)),
        input_output_aliases={i: 2 + i for i in range(2 * ns)},
        compiler_params=pltpu.CompilerParams(has_side_effects=EFFECT, collective_id=barrier_id),
    )(*[_in_hbm(a) for a in grads], *[_in_hbm(a) for a in zones])
    return out[0], out[1], out[2:2 + ns], out[2 + ns:2 + 2 * ns], out[-1]


def _pair_exchange_wait(sections, send_sems, recv_sems, grads, zones, after, name):
    ns = len(sections)

    def body(*refs):
        for cp in _pair_copies(sections, refs[:ns], refs[ns:2 * ns], refs[2 * ns], refs[2 * ns + 1]):
            cp.wait_send()
            cp.wait_recv()

    out = pl.pallas_call(
        body, name=name,
        out_shape=tuple(pltpu.HBM(a.shape, a.dtype) for a in (*grads, *zones)),
        in_specs=[HBM] * (2 * ns) + [SEM, SEM, ANY],
        out_specs=(HBM,) * (2 * ns),
        input_output_aliases={i: i for i in range(2 * ns)},
        compiler_params=pltpu.CompilerParams(has_side_effects=EFFECT),
    )(*grads, *zones, send_sems, recv_sems, after)
    return out[:ns], out[ns:]


def _pair_add(sections, grads, got, core, name):
    ns = len(sections)

    def body(core_ref, *refs):
        g_refs, got_refs, p_refs = refs[:ns], refs[ns:2 * ns], refs[2 * ns:]
        for s in range(ns):
            p_refs[s][0] = (g_refs[s][...].astype(F32) + got_refs[s][0].astype(F32)).astype(BF16)

    slot = [pl.BlockSpec((1, rows, cols), lambda k, c: (k, 0, 0)) for _, rows, cols in sections]
    return pl.pallas_call(
        body, name=name,
        out_shape=tuple(jax.ShapeDtypeStruct((N_CHIPS, rows, cols), BF16) for _, rows, cols in sections),
        grid_spec=pltpu.PrefetchScalarGridSpec(
            num_scalar_prefetch=1, grid=(N_CHIPS,),
            in_specs=[pl.BlockSpec((rows, cols), lambda k, c: (2 * k + c[0], 0)) for _, rows, cols in sections]
            + slot,
            out_specs=tuple(slot)),
        compiler_params=_params(dimension_semantics=("parallel",)),
    )(core, *grads, *got)


def _chip_copies(sections, p_refs, land, send_sems, recv_sems):
    ns = len(sections)
    x, y, c = _position()
    return [pltpu.make_async_remote_copy(
        src_ref=p_refs[s].at[2 * cx + cy], dst_ref=land[s].at[j],
        send_sem=send_sems.at[j * ns + s], recv_sem=recv_sems.at[j * ns + s],
        device_id=(cx, cy, c), device_id_type=MESH_ID)
        for j, (cx, cy) in enumerate(_other_chips(x, y)) for s in range(ns)]


def _chip_exchange(sections, parts, name):
    ns = len(sections)

    def body(*refs):
        copies = _chip_copies(sections, refs[:ns], refs[ns:2 * ns], *refs[2 * ns:])
        for cp in copies:
            cp.start()
        for cp in copies:
            cp.wait_recv()
        for cp in copies:
            cp.wait_send()

    n = 3 * ns
    return pl.pallas_call(
        body, name=name,
        out_shape=tuple(jax.ShapeDtypeStruct((3, rows, cols), BF16) for _, rows, cols in sections),
        in_specs=[ANY] * ns, out_specs=(ANY,) * ns,
        scratch_shapes=[pltpu.SemaphoreType.DMA((n,)), pltpu.SemaphoreType.DMA((n,))],
    )(*parts)


def _chip_exchange_start(sections, parts, name):
    ns = len(sections)

    def body(*refs):
        p_refs, land = refs[:ns], refs[ns:2 * ns]
        send_sems, recv_sems = refs[2 * ns], refs[2 * ns + 1]
        token = refs[-1]
        for cp in _chip_copies(sections, p_refs, land, send_sems, recv_sems):
            cp.start()
        token[...] = jnp.zeros_like(token)

    zones = [lax.empty((3, rows, cols), BF16) for _, rows, cols in sections]
    out = pl.pallas_call(
        body, name=name,
        out_shape=(pltpu.SemaphoreType.DMA((3 * ns,)), pltpu.SemaphoreType.DMA((3 * ns,)),
                   *[pltpu.HBM(a.shape, a.dtype) for a in parts], *[pltpu.HBM(a.shape, a.dtype) for a in zones],
                   jax.ShapeDtypeStruct((8, LANES), F32)),
        in_specs=[HBM] * (2 * ns),
        out_specs=(SEM, SEM, *[HBM] * (2 * ns), pl.BlockSpec(memory_space=pltpu.VMEM)),
        input_output_aliases={i: 2 + i for i in range(2 * ns)},
        compiler_params=pltpu.CompilerParams(has_side_effects=EFFECT),
    )(*[_in_hbm(a) for a in parts], *[_in_hbm(a) for a in zones])
    return out[0], out[1], out[2:2 + ns], out[2 + ns:2 + 2 * ns], out[-1]


def _chip_exchange_wait(sections, send_sems, recv_sems, parts, zones, after, name):
    ns = len(sections)

    def body(*refs):
        p_refs, land = refs[:ns], refs[ns:2 * ns]
        for cp in _chip_copies(sections, p_refs, land, refs[2 * ns], refs[2 * ns + 1]):
            cp.wait_send()
            cp.wait_recv()

    out = pl.pallas_call(
        body, name=name,
        out_shape=tuple(pltpu.HBM(a.shape, a.dtype) for a in (*parts, *zones)),
        in_specs=[HBM] * (2 * ns) + [SEM, SEM, ANY],
        out_specs=(HBM,) * (2 * ns),
        input_output_aliases={i: i for i in range(2 * ns)},
        compiler_params=pltpu.CompilerParams(has_side_effects=EFFECT),
    )(*parts, *zones, send_sems, recv_sems, after)
    return out[:ns], out[ns:]


def _grad_finish(sections, parts, far, chip, name):
    ns = len(sections)

    def body(chip_ref, *refs):
        p_refs, b_refs, g_refs = refs[:ns], refs[ns:2 * ns], refs[2 * ns:]
        for s in range(ns):
            g = p_refs[s][0].astype(F32)
            for j in range(3):
                g = g + b_refs[s][j].astype(F32)
            g_refs[s][...] = g

    half = [(rows // 2, cols) for _, rows, cols in sections]
    return pl.pallas_call(
        body, name=name,
        out_shape=tuple(jax.ShapeDtypeStruct((rows, cols), F32) for _, rows, cols in sections),
        grid_spec=pltpu.PrefetchScalarGridSpec(
            num_scalar_prefetch=1, grid=(2,),
            in_specs=[pl.BlockSpec((1, r, c), lambda i, chip: (chip[0], i, 0)) for r, c in half]
            + [pl.BlockSpec((3, r, c), lambda i, chip: (0, i, 0)) for r, c in half],
            out_specs=tuple(pl.BlockSpec((r, c), lambda i, chip: (i, 0)) for r, c in half)),
        compiler_params=_params(dimension_semantics=("parallel",)),
    )(chip, *parts, *far)


def _sum_devices(parts, rows, name):
    cols = parts.shape[1]
    tr = rows // 2

    def body(*refs):
        s = refs[0][...].astype(F32)
        for d in range(1, N_DEV):
            s = s + refs[d][...].astype(F32)
        refs[N_DEV][...] = s

    return pl.pallas_call(
        body, name=name,
        out_shape=jax.ShapeDtypeStruct((rows, cols), F32),
        grid=(2,),
        in_specs=[pl.BlockSpec((tr, cols), lambda i, d=d: (2 * d + i, 0)) for d in range(N_DEV)],
        out_specs=pl.BlockSpec((tr, cols), lambda i: (i, 0)),
        compiler_params=_params(dimension_semantics=("parallel",)),
    )(*([parts] * N_DEV))


def _adamw_step(w_ref, g_ref, m_ref, v_ref, d_ref, nm_ref, nv_ref):
    c1 = 1.0 / (1.0 - ADAM_B1 ** ADAM_STEP)
    c2 = 1.0 / (1.0 - ADAM_B2 ** ADAM_STEP)
    gv = g_ref[...]
    nm = ADAM_B1 * m_ref[...] + (1.0 - ADAM_B1) * gv
    nv = ADAM_B2 * v_ref[...] + (1.0 - ADAM_B2) * (gv * gv)
    nm_ref[...] = nm
    nv_ref[...] = nv
    d_ref[...] = (-ADAM_LR) * ((nm * c1) / (jnp.sqrt(nv * c2) + ADAM_EPS) + ADAM_WD * w_ref[...])


def _adamw_small(params, name):
    n = len(params)

    def body(*refs):
        for k in range(n):
            _adamw_step(*refs[4 * k:4 * k + 4], *refs[4 * n + 3 * k:4 * n + 3 * k + 3])

    out = pl.pallas_call(
        body, name=name,
        out_shape=tuple(jax.ShapeDtypeStruct(p[0].shape, F32) for p in params for _ in range(3)),
    )(*[a for p in params for a in p])
    return [out[3 * k:3 * k + 3] for k in range(n)]


def _adamw(w, g, m, v, name):
    rows, cols = w.shape
    tr = rows
    while tr * cols * 4 > (1 << 20) and tr % 16 == 0:
        tr //= 2

    def body(*refs):
        _adamw_step(*refs)

    spec = pl.BlockSpec((tr, cols), lambda i: (i, 0))
    shape = jax.ShapeDtypeStruct((rows, cols), F32)
    return pl.pallas_call(
        body, name=name,
        out_shape=(shape, shape, shape),
        grid=(rows // tr,),
        in_specs=[spec] * 4, out_specs=(spec,) * 3,
        compiler_params=_params(dimension_semantics=("parallel",)),
    )(w, g, m, v)


NAMES = ("ln1_g", "w_in", "b_in", "rpb", "w_att_o", "conv_w", "conv_b", "w_rg_a", "b_rg_a", "w_rg_i",
         "b_rg_i", "lru_lambda", "w_rec_o", "w_out", "ln2_g", "w_ff1", "w_ff2", "lnf_g")
TRANSPOSED = {"w_in": "w_in_t", "w_att_o": "w_att_o_t", "w_ff1": "w_ff1_t"}
ROW_SHARDED = ("w_rec_o", "w_out", "w_ff2")
REPLICATED = (("ln1_g", (1, D)), ("b_in", (1, D_IN)), ("rpb", (N_HEADS * N_RPB_R, N_RPB_C)),
              ("conv_b", (1, D_REC)), ("w_rg_a", (2 * N_REC_BLOCKS * REC_BLOCK, REC_BLOCK)),
              ("w_rg_i", (2 * N_REC_BLOCKS * REC_BLOCK, REC_BLOCK)), ("ln2_g", (1, D)), ("lnf_g", (1, D)))
GATE_BLOCKS = ("w_rg_a", "w_rg_i")
SMALL_ROWS = 112


def _chan_bits(vectors):
    chan = jnp.concatenate(vectors, axis=0)
    bits = lax.bitcast_convert_type(chan, BF16).reshape(-1)
    return jnp.pad(bits, (0, CHAN_BLOCK_ROWS * D - bits.shape[0])).reshape(CHAN_BLOCK_ROWS, D)


def _chan_from_bits(gathered):
    bits = gathered.reshape(N_DEV, CHAN_BLOCK_ROWS * D)[:, :2 * N_CHAN_ROWS * LANES]
    chan = lax.bitcast_convert_type(bits.reshape(N_DEV, N_CHAN_ROWS, LANES, 2), F32)
    return chan.transpose(1, 0, 2).reshape(N_CHAN_ROWS, D)


def kernel(x, ln1_g, w_in, b_in, rpb, w_att_o, conv_w, conv_b, w_rg_a, b_rg_a, w_rg_i, b_rg_i, lru_lambda, w_rec_o, w_out, ln2_g, w_ff1, w_ff2, lnf_g, loss_target, m_ln1_g, m_w_in, m_b_in, m_rpb, m_w_att_o, m_conv_w, m_conv_b, m_w_rg_a, m_b_rg_a, m_w_rg_i, m_b_rg_i, m_lru_lambda, m_w_rec_o, m_w_out, m_ln2_g, m_w_ff1, m_w_ff2, m_lnf_g, v_ln1_g, v_w_in, v_b_in, v_rpb, v_w_att_o, v_conv_w, v_conv_b, v_w_rg_a, v_b_rg_a, v_w_rg_i, v_b_rg_i, v_lru_lambda, v_w_rec_o, v_w_out, v_ln2_g, v_w_ff1, v_w_ff2, v_lnf_g):
    w = dict(zip(NAMES, (ln1_g, w_in, b_in, rpb, w_att_o, conv_w, conv_b, w_rg_a, b_rg_a, w_rg_i,
                         b_rg_i, lru_lambda, w_rec_o, w_out, ln2_g, w_ff1, w_ff2, lnf_g)))
    m = dict(zip(NAMES, (m_ln1_g, m_w_in, m_b_in, m_rpb, m_w_att_o, m_conv_w, m_conv_b, m_w_rg_a,
                         m_b_rg_a, m_w_rg_i, m_b_rg_i, m_lru_lambda, m_w_rec_o, m_w_out, m_ln2_g,
                         m_w_ff1, m_w_ff2, m_lnf_g)))
    v = dict(zip(NAMES, (v_ln1_g, v_w_in, v_b_in, v_rpb, v_w_att_o, v_conv_w, v_conv_b, v_w_rg_a,
                         v_b_rg_a, v_w_rg_i, v_b_rg_i, v_lru_lambda, v_w_rec_o, v_w_out, v_ln2_g,
                         v_w_ff1, v_w_ff2, v_lnf_g)))
    xi, yi, ci = _position()

    shard = {t: w[n][0].T.astype(BF16) for n, t in TRANSPOSED.items()}
    shard.update({n: w[n][0].astype(BF16) for n in ROW_SHARDED})
    shard["chan"] = _chan_bits([w[n][0] for n, _ in CHAN])
    first, later = ("w_in_t", "chan"), ("w_rec_o", "w_out", "w_att_o_t", "w_ff1_t", "w_ff2")
    *gathered, done = _all_gather([shard[n] for n in first], "weight_all_gather")
    p = dict(zip(first, gathered))
    send_sems, recv_sems, sent, zones, token = _gather_start([shard[n] for n in later], done,
                                                             "weight_gather_start")

    travelling = {"shards": sent, "zones": zones}
    stages = (("w_rec_o", "w_out", "w_att_o_t"), ("w_ff1_t", "w_ff2"))

    def late_weights(after, stage):
        which = [later.index(n) for n in stages[stage]]
        travelling["shards"], travelling["zones"] = _gather_wait(
            send_sems, recv_sems, travelling["shards"], travelling["zones"], which, after,
            "weight_gather_wait_%d" % stage)
        return dict(zip(stages[stage], _gather_pass_on(
            [shard[n].shape[0] for n in stages[stage]], [travelling["zones"][i] for i in which],
            PASS_ON_IDS[stage], "weight_gather_pass_on_%d" % stage)))

    chan = _chan_from_bits(p.pop("chan"))
    r0 = 0
    for n, rows in CHAN:
        p[n] = chan[r0:r0 + rows]
        r0 += rows
    p.update(ln1_g=w["ln1_g"], b_in=w["b_in"] + token[0, 0], rpb=w["rpb"][0], conv_b=w["conv_b"],
             w_rg_a=w["w_rg_a"][0], w_rg_i=w["w_rg_i"][0], ln2_g=w["ln2_g"],
             lnf_g=w["lnf_g"].reshape(1, D))

    core = jnp.reshape(ci, (1,)).astype(jnp.int32)
    chip = jnp.reshape(2 * xi + yi, (1,)).astype(jnp.int32)
    early_sections, late_sections = SECTIONS[1:], SECTIONS[:1]
    in_flight = {}

    def pair_sum_and_send(group, sections, after):
        send_sems, recv_sems, sect, zones, _ = in_flight["pair_" + group]
        sect, got = _pair_exchange_wait(sections, send_sems, recv_sems, sect, zones, after,
                                        "grad_pair_exchange_wait_" + group)
        parts = _pair_add(sections, sect, got, core, "grad_pair_add_" + group)
        in_flight[group] = _chip_exchange_start(sections, parts, "grad_chip_exchange_start_" + group)
        return in_flight[group][-1]

    def reduce_early(grads):
        chan_g = jnp.concatenate([grads[n] for n, _ in CHAN], axis=0)
        chan_g = chan_g.reshape(N_CHAN_ROWS, N_DEV, LANES).transpose(1, 0, 2).astype(BF16)
        chan_g = jnp.pad(chan_g.reshape(N_DEV, -1), ((0, 0), (0, CHAN_BLOCK_ROWS * D - N_CHAN_ROWS * LANES)))
        grads["chan"] = chan_g.reshape(N_DEV * CHAN_BLOCK_ROWS, D)
        grads["gates"] = jnp.concatenate([grads[n].reshape(-1, D) for n in GATE_BLOCKS], axis=0).astype(BF16)
        in_flight["pair_early"] = _pair_exchange_start(
            early_sections, [grads[n] for n, _, _ in early_sections], PAIR_EARLY_ID,
            "grad_pair_exchange_start_early")
        return pair_sum_and_send("early", early_sections, in_flight["pair_early"][-1])[0, 0]

    loss_part, grad_x, grads = _local_step(x[0], loss_target[0], p, late_weights, reduce_early)
    in_flight["pair_late"] = _pair_exchange_start(
        late_sections, [grads[n] for n, _, _ in late_sections], PAIR_LATE_ID, "grad_pair_exchange_start_late")

    def finish(group, sections, after, name):
        send_sems, recv_sems, parts, zones, _ = in_flight[group]
        parts, far = _chip_exchange_wait(sections, send_sems, recv_sems, parts, zones, after,
                                         "grad_chip_exchange_wait_" + name)
        return dict(zip((n for n, _, _ in sections),
                        _grad_finish(sections, parts, far, chip, "grad_finish_" + name)))

    summed = finish("early", early_sections, in_flight["pair_late"][-1], "early")
    started_late = pair_sum_and_send("late", late_sections, summed["gates"])

    flat = jnp.concatenate([grads[n].reshape(-1) for n, _ in REPLICATED if n not in GATE_BLOCKS]
                           + [loss_part.reshape(-1) + started_late[0, 0]])
    n_small = flat.shape[0]
    flat = jnp.pad(flat, (0, SMALL_ROWS * LANES - n_small)).reshape(SMALL_ROWS, LANES)
    small_parts, gate_sum, _ = _all_gather([flat, summed["gates"]], "small_grad_all_gather")
    small = _sum_devices(small_parts, SMALL_ROWS, "small_grad_sum").reshape(-1)
    loss = small[n_small - 1]

    g, delta, new_m, new_v = {}, {}, {}, {}

    def update(n, g2, shape2):
        d2, m2, v2 = _adamw(w[n].reshape(shape2), g2, m[n].reshape(shape2), v[n].reshape(shape2),
                            "adamw_" + n)
        g[n], delta[n], new_m[n], new_v[n] = (a.reshape(w[n].shape) for a in (g2, d2, m2, v2))

    small_params = []
    o = 0
    for n, shape2 in REPLICATED:
        if n in GATE_BLOCKS:
            k, rows = GATE_BLOCKS.index(n), gate_sum.shape[0] // len(GATE_BLOCKS)
            update(n, gate_sum[k * rows:(k + 1) * rows].reshape(shape2), shape2)
        else:
            size = shape2[0] * shape2[1]
            small_params.append((n, small[o:o + size].reshape(shape2), shape2))
            o += size
    chan_back = summed["chan"].reshape(-1)[:N_CHAN_ROWS * LANES].reshape(N_CHAN_ROWS, LANES)
    r0 = 0
    for n, rows in CHAN:
        small_params.append((n, chan_back[r0:r0 + rows], (rows, LANES)))
        r0 += rows
    results = _adamw_small([(w[n].reshape(s2), g2, m[n].reshape(s2), v[n].reshape(s2))
                            for n, g2, s2 in small_params], "adamw_vectors")
    for (n, g2, _), (d2, m2, v2) in zip(small_params, results):
        g[n], delta[n], new_m[n], new_v[n] = (a.reshape(w[n].shape) for a in (g2, d2, m2, v2))

    for n in ROW_SHARDED:
        update(n, summed[n], summed[n].shape)
    for n, t in TRANSPOSED.items():
        if t in summed:
            update(n, summed[t].T, summed[t].shape[::-1])
    summed = finish("late", late_sections, _after_all(list(delta.values()), "updates_done"), "late")
    update("w_in", summed["w_in_t"].T, summed["w_in_t"].shape[::-1])

    return (loss, grad_x[None], *[g[n] for n in NAMES], *[delta[n] for n in NAMES],
            *[new_m[n] for n in NAMES], *[new_v[n] for n in NAMES])
```

```python
import math

import numpy as np
import jax
import jax.numpy as jnp
from jax import lax
from jax.experimental import pallas as pl
from jax.experimental.pallas import tpu as pltpu

F32 = jnp.float32
BF16 = jnp.bfloat16

T = 2048
D = 1024
D_ATT = 512
D_REC = 1024
D_FF = 4096
D_IN = 5632
N_HEADS = 8
DH = 64
GRID_W = 64
ROWS = T // GRID_W
WIN_H = 8
WIN_W = 16
KWIN = WIN_H * GRID_W
N_RPB_R = 2 * WIN_H - 1
N_RPB_C = 2 * WIN_W - 1
N_REC_BLOCKS = 16
REC_BLOCK = 64
CG = 128
N_CG = D_REC // CG
LRU_C = 8.0
EPS = 1e-6
N_DEV = 8
N_CHIPS = 4
LANES = 128

ADAM_LR = 0.001
ADAM_B1 = 0.9
ADAM_B2 = 0.999
ADAM_EPS = 1e-08
ADAM_WD = 0.01
ADAM_STEP = 10

MESH_AXES = ("x", "y", "c")
VMEM_LIMIT = 56 * 1024 * 1024

TILE = 512
DZ_ARRAYS = ((0, 3, 1), (3, 4, 2), (7, 4, 2))
N_DZ_TILES = D_IN // TILE


def _params(**kw):
    return pltpu.CompilerParams(vmem_limit_bytes=VMEM_LIMIT, **kw)


HG = 4
HQ = HG * GRID_W
HC = HG * DH


def _att_tables():
    rq = np.arange(GRID_W)
    kc = np.arange(KWIN) % GRID_W
    win_start = np.clip(rq - WIN_W // 2, 0, GRID_W - WIN_W)
    valid = (kc[None, :] >= win_start[:, None]) & (kc[None, :] < win_start[:, None] + WIN_W)
    same_head = (np.arange(HQ)[:, None] // GRID_W) == (np.arange(HC)[None, :] // DH)
    return valid.astype(np.float32), same_head.astype(np.float32)


def _pair_mask():
    half = np.arange(2 * DH) // DH
    return (half[:, None] == half[None, :]).astype(np.float32)


def _dup_table():
    return np.concatenate([np.eye(REC_BLOCK, dtype=np.float32)] * 2, axis=1)


def _sigmoid(x):
    return 0.5 * jnp.tanh(0.5 * x) + 0.5


def _softplus(x):
    return jnp.maximum(x, 0.0) + jnp.log(1.0 + jnp.exp(-jnp.abs(x)))


def _one_minus_square(log_a, a):
    x = 2.0 * log_a
    series = -x * (1.0 + x * (0.5 + x * (1.0 / 6.0)))
    return jnp.where(x > -0.02, series, 1.0 - a * a)


_GELU_C = math.sqrt(2.0 / math.pi)


def _gelu_and_grad(x):
    x2 = x * x
    inner = _GELU_C * (x + 0.044715 * x * x2)
    t = jnp.tanh(inner)
    g = 0.5 * x * (1.0 + t)
    dg = 0.5 * (1.0 + t) + 0.5 * x * (1.0 - t * t) * _GELU_C * (1.0 + 3.0 * 0.044715 * x2)
    return g, dg


def _dot(a, b):
    return jnp.dot(a, b, preferred_element_type=F32)


def _dot_nt(a, b):
    return lax.dot_general(a, b, (((1,), (1,)), ((), ())), preferred_element_type=F32)


def _dot_tn(a, b):
    return lax.dot_general(a, b, (((0,), (0,)), ((), ())), preferred_element_type=F32)


def _dot_exact(a, b):
    return jnp.dot(a, b, precision=lax.Precision.HIGHEST, preferred_element_type=F32)


def _shift_rows(x, s):
    n = x.shape[0]
    rows = lax.broadcasted_iota(jnp.int32, x.shape, 0)
    y = pltpu.roll(x, s % n, 0)
    if s > 0:
        return jnp.where(rows >= s, y, 0.0)
    return jnp.where(rows < n + s, y, 0.0)


def _rms_bwd(dh, xh, r, g):
    dxh = dh * g
    return r * (dxh - xh * jnp.mean(dxh * xh, axis=-1, keepdims=True))


def _matmul(a, b, mode, out_dtype, name, tm=512, tn=1024, tk=2048):
    if mode == "nn":
        (m, k), (k2, n) = a.shape, b.shape
    elif mode == "nt":
        (m, k), (n, k2) = a.shape, b.shape
    else:
        (k, m), (k2, n) = a.shape, b.shape
    assert k == k2
    tm, tn, tk = min(tm, m), min(tn, n), min(tk, k)
    assert m % tm == 0 and n % tn == 0 and k % tk == 0
    nk = k // tk
    dot = {"nn": _dot, "nt": _dot_nt, "tn": _dot_tn}[mode]

    def body(a_ref, b_ref, o_ref, acc):
        kk = pl.program_id(2)
        part = dot(a_ref[...].astype(BF16), b_ref[...].astype(BF16))
        if nk == 1:
            o_ref[...] = part.astype(out_dtype)
            return

        @pl.when(kk == 0)
        def _():
            acc[...] = part

        @pl.when(kk > 0)
        def _():
            acc[...] += part

        @pl.when(kk == nk - 1)
        def _():
            o_ref[...] = acc[...].astype(out_dtype)

    if mode == "tn":
        a_spec = pl.BlockSpec((tk, tm), lambda i, j, kk: (kk, i))
    else:
        a_spec = pl.BlockSpec((tm, tk), lambda i, j, kk: (i, kk))
    if mode == "nt":
        b_spec = pl.BlockSpec((tn, tk), lambda i, j, kk: (j, kk))
    else:
        b_spec = pl.BlockSpec((tk, tn), lambda i, j, kk: (kk, j))
    return pl.pallas_call(
        body, name=name,
        out_shape=jax.ShapeDtypeStruct((m, n), out_dtype),
        grid=(m // tm, n // tn, nk),
        in_specs=[a_spec, b_spec],
        out_specs=pl.BlockSpec((tm, tn), lambda i, j, kk: (i, j)),
        scratch_shapes=[pltpu.VMEM((tm, tn) if nk > 1 else (8, LANES), F32)],
        compiler_params=_params(dimension_semantics=("parallel", "parallel", "arbitrary")),
    )(a, b)


def _in_proj(x, g1, w_in_t, b_in):
    tm = 512

    def body(x_ref, g_ref, w_hbm, b_ref, qkv_ref, uy_ref, gg_ref, h_ref, w):
        @pl.when(pl.program_id(0) == 0)
        def _():
            pltpu.sync_copy(w_hbm, w)

        xv = x_ref[...]
        r = lax.rsqrt(jnp.mean(xv * xv, axis=-1, keepdims=True) + EPS)
        h = ((xv * r) * g_ref[...]).astype(BF16)
        h_ref[...] = h
        row0 = 0
        for ref in (qkv_ref, uy_ref, gg_ref):
            for c0 in range(0, ref.shape[1], TILE):
                z = _dot_nt(h, w[row0:row0 + TILE, :]) + b_ref[:, row0:row0 + TILE]
                ref[:, c0:c0 + TILE] = z.astype(ref.dtype)
                row0 += TILE

    tok = lambda width: pl.BlockSpec((tm, width), lambda i: (i, 0))
    return pl.pallas_call(
        body, name="in_proj",
        out_shape=(jax.ShapeDtypeStruct((T, 3 * D_ATT), BF16),
                   jax.ShapeDtypeStruct((T, 2 * D_REC), F32),
                   jax.ShapeDtypeStruct((T, 2 * D), F32),
                   jax.ShapeDtypeStruct((T, D), BF16)),
        grid=(T // tm,),
        in_specs=[tok(D), pl.BlockSpec((1, D), lambda i: (0, 0)), pl.BlockSpec(memory_space=pl.ANY),
                  pl.BlockSpec((1, D_IN), lambda i: (0, 0))],
        out_specs=(tok(3 * D_ATT), tok(2 * D_REC), tok(2 * D), tok(D)),
        scratch_shapes=[pltpu.VMEM((D_IN, D), BF16)],
        compiler_params=_params(dimension_semantics=("arbitrary",)),
    )(x, g1, w_in_t, b_in)


def _dz_specs(rows, tile_of, row_of):
    def spec(off, n, per_plane):
        def index(*ids):
            t = jnp.clip(tile_of(*ids) - off, 0, n - 1)
            return (t // per_plane, row_of(*ids), t % per_plane)
        return pl.BlockSpec((1, rows, TILE), index)
    return [spec(off, n, per) for off, n, per in DZ_ARRAYS]


def _dh_norm1_bwd(dz, w_in_t, x, g1, dx1):
    tm = 512

    def body(dqkv_ref, duy_ref, dgg_ref, w_hbm, x_ref, g_ref, dx1_ref, gx_ref, dg_ref, w):
        @pl.when(pl.program_id(0) == 0)
        def _():
            pltpu.sync_copy(w_hbm, w)
            dg_ref[...] = jnp.zeros_like(dg_ref)

        dh, row0 = None, 0
        for ref in (dqkv_ref, duy_ref, dgg_ref):
            for plane in range(ref.shape[0]):
                cols = ref.shape[2]
                part = _dot(ref[plane], w[row0:row0 + cols, :])
                dh = part if dh is None else dh + part
                row0 += cols
        xv = x_ref[...]
        r = lax.rsqrt(jnp.mean(xv * xv, axis=-1, keepdims=True) + EPS)
        xh = xv * r
        dg_ref[...] += jnp.sum(dh * xh, axis=0, keepdims=True)
        gx_ref[...] = dx1_ref[...] + _rms_bwd(dh, xh, r, g_ref[...])

    tok = pl.BlockSpec((tm, D), lambda i: (i, 0))
    vec = pl.BlockSpec((1, D), lambda i: (0, 0))
    planes = lambda a: pl.BlockSpec((a.shape[0], tm, a.shape[2]), lambda i: (0, i, 0))
    return pl.pallas_call(
        body, name="dh_norm1_bwd",
        out_shape=(jax.ShapeDtypeStruct((T, D), F32), jax.ShapeDtypeStruct((1, D), F32)),
        grid=(T // tm,),
        in_specs=[planes(a) for a in dz] + [pl.BlockSpec(memory_space=pl.ANY), tok, vec, tok],
        out_specs=(tok, vec),
        scratch_shapes=[pltpu.VMEM((D_IN, D), BF16)],
        compiler_params=_params(dimension_semantics=("arbitrary",)),
    )(*dz, w_in_t, x, g1, dx1)


def _grad_w_in(dz, h):
    def body(*refs):
        seg_refs = refs[:3]
        h_ref, gw_ref, gb_ref = refs[3:]
        j = pl.program_id(0)

        for s, (off, n, _) in enumerate(DZ_ARRAYS):
            @pl.when((j >= off) & (j < off + n))
            def _(s=s):
                a = seg_refs[s][0]
                gw_ref[...] = _dot_tn(a, h_ref[...]).astype(BF16)
                gb_ref[...] = jnp.sum(a.astype(F32), axis=0, keepdims=True)

    return pl.pallas_call(
        body, name="grad_w_in",
        out_shape=(jax.ShapeDtypeStruct((D_IN, D), BF16), jax.ShapeDtypeStruct((1, D_IN), F32)),
        grid=(N_DZ_TILES,),
        in_specs=_dz_specs(T, lambda j: j, lambda j: 0) + [pl.BlockSpec((T, D), lambda j: (0, 0))],
        out_specs=(pl.BlockSpec((TILE, D), lambda j: (j, 0)), pl.BlockSpec((1, TILE), lambda j: (0, j))),
        compiler_params=_params(dimension_semantics=("parallel",)),
    )(*dz, h)


def _rpb_rows(rpb):
    padded = jnp.pad(rpb, ((0, 0), (0, 0), (0, GRID_W - N_RPB_C)))
    rows = [padded[:, WIN_H - 1 - oi: 2 * WIN_H - 1 - oi].reshape(N_HEADS // HG, HG, KWIN)
            for oi in range(WIN_H)]
    return jnp.stack(rows, axis=0)


SKEW = KWIN - (WIN_W - 1)


MASKED = -1e30


def _bias_tiles(rows_ref, valid, bias_s):
    for oi in range(WIN_H):
        for hh in range(HG):
            row = jnp.broadcast_to(rows_ref[oi, 0, hh:hh + 1, :], (GRID_W, KWIN))
            tile = pltpu.roll(row, SKEW, 1, stride=1, stride_axis=0)
            bias_s[oi, hh * GRID_W:(hh + 1) * GRID_W, :] = jnp.where(valid, tile, MASKED)


def _bias_tile_grads(gb_s, flip, out_ref):
    for oi in range(WIN_H):
        for hh in range(HG):
            g = _dot_exact(flip, gb_s[oi, hh * GRID_W:(hh + 1) * GRID_W, :])
            back = pltpu.roll(g, KWIN - (GRID_W - WIN_W), 1, stride=1, stride_axis=0)
            out_ref[0, oi, hh:hh + 1, :] = jnp.sum(back, axis=0, keepdims=True)


def _rpb_fold(row_grads):
    g = row_grads.transpose(1, 0, 2, 3).reshape(WIN_H, N_HEADS, WIN_H, GRID_W)
    g = g.transpose(0, 2, 1, 3)

    def body(g_ref, o_ref):
        for dr in range(N_RPB_R):
            terms = [g_ref[oi, i] for oi in range(WIN_H) for i in range(WIN_H) if i - oi + WIN_H - 1 == dr]
            acc = terms[0]
            for term in terms[1:]:
                acc = acc + term
            o_ref[dr] = acc

    out = pl.pallas_call(
        body, name="rpb_fold",
        out_shape=jax.ShapeDtypeStruct((N_RPB_R, N_HEADS, GRID_W), F32),
    )(g)
    return out.transpose(1, 0, 2)[:, :, :N_RPB_C]


ATT_GROUPS = N_HEADS // HG
ATT_UNROLL = 4


def _stacked(rows64, same_head):
    return jnp.where(same_head, jnp.concatenate([rows64] * HG, axis=0), jnp.zeros((), BF16))


def _own_heads(stacked):
    head = lax.broadcasted_iota(jnp.int32, (GRID_W, HC), 1) // DH
    out = stacked[:GRID_W]
    for h in range(1, HG):
        out = jnp.where(head == h, stacked[h * GRID_W:(h + 1) * GRID_W], out)
    return out


def _att_scores(q_ref, k_ref, bias_ref, same_head, r):
    rs = jnp.clip(r - WIN_H // 2, 0, ROWS - WIN_H)
    oi = r - rs
    q0 = pl.multiple_of(r * GRID_W, GRID_W)
    k0 = pl.multiple_of(rs * GRID_W, GRID_W)
    q2 = _stacked(q_ref[pl.ds(q0, GRID_W), :] * (DH ** -0.5), same_head)
    kw = k_ref[pl.ds(k0, KWIN), :]
    s = _dot_nt(q2, kw) + bias_ref[oi]
    e = jnp.exp(s - jnp.max(s, axis=-1, keepdims=True))
    return e, 1.0 / jnp.sum(e, axis=-1, keepdims=True), q2, kw, q0, k0, oi


def _att_specs():
    col = lambda off: pl.BlockSpec((T, HC), lambda g: (0, g + off * ATT_GROUPS))
    tables = [pl.BlockSpec((WIN_H, 1, HG, KWIN), lambda g: (0, g, 0, 0)),
              pl.BlockSpec((GRID_W, KWIN), lambda g: (0, 0)),
              pl.BlockSpec((HQ, HC), lambda g: (0, 0))]
    return col, tables, pltpu.VMEM((WIN_H, HQ, KWIN), F32)


def _att_fwd(qkv, bias_rows):
    valid_np, same_head_np = _att_tables()

    def body(q_ref, k_ref, v_ref, rows_ref, valid_ref, head_ref, o_ref, bias_s):
        same_head = head_ref[...] > 0.5
        _bias_tiles(rows_ref, valid_ref[...] > 0.5, bias_s)

        def row(r, carry):
            e, rl, _, _, q0, k0, _ = _att_scores(q_ref, k_ref, bias_s, same_head, r)
            o2 = _dot((e * rl).astype(BF16), v_ref[pl.ds(k0, KWIN), :])
            o_ref[pl.ds(q0, GRID_W), :] = _own_heads(o2).astype(BF16)
            return carry

        lax.fori_loop(0, ROWS, row, 0, unroll=ATT_UNROLL)

    col, tables, tiles = _att_specs()
    return pl.pallas_call(
        body, name="att_fwd",
        out_shape=jax.ShapeDtypeStruct((T, D_ATT), BF16),
        grid=(ATT_GROUPS,),
        in_specs=[col(0), col(1), col(2)] + tables,
        out_specs=col(0),
        scratch_shapes=[tiles],
        compiler_params=_params(dimension_semantics=("parallel",)),
    )(qkv, qkv, qkv, bias_rows, jnp.asarray(valid_np), jnp.asarray(same_head_np))


def _att_bwd(qkv, bias_rows, datt, after):
    valid_np, same_head_np = _att_tables()

    def body(q_ref, k_ref, v_ref, do_ref, rows_ref, valid_ref, head_ref, flip_ref,
             dqkv_ref, grows_ref, dk_acc, dv_acc, bias_s, gb_s):
        same_head = head_ref[...] > 0.5
        dk_acc[...] = jnp.zeros_like(dk_acc)
        dv_acc[...] = jnp.zeros_like(dv_acc)
        gb_s[...] = jnp.zeros_like(gb_s)
        _bias_tiles(rows_ref, valid_ref[...] > 0.5, bias_s)

        def row(r, carry):
            e, rl, q2, kw, q0, k0, oi = _att_scores(q_ref, k_ref, bias_s, same_head, r)
            do2 = _stacked(do_ref[pl.ds(q0, GRID_W), :], same_head)
            vw = v_ref[pl.ds(k0, KWIN), :]
            p = e * rl
            dp = _dot_nt(do2, vw)
            ds = p * (dp - jnp.sum(dp * p, axis=-1, keepdims=True))
            p16 = p.astype(BF16)
            ds16 = ds.astype(BF16)
            dv_acc[pl.ds(k0, KWIN), :] += _dot_tn(p16, do2)
            dk_acc[pl.ds(k0, KWIN), :] += _dot_tn(ds16, q2)
            dq2 = _dot(ds16, kw) * (DH ** -0.5)
            dqkv_ref[0, pl.ds(q0, GRID_W), :] = _own_heads(dq2).astype(BF16)
            gb_s[oi] += ds
            return carry

        lax.fori_loop(0, ROWS, row, 0, unroll=ATT_UNROLL)
        dqkv_ref[1] = dk_acc[...].astype(BF16)
        dqkv_ref[2] = dv_acc[...].astype(BF16)
        _bias_tile_grads(gb_s, flip_ref[...], grows_ref)

    col, tables, tiles = _att_specs()
    return pl.pallas_call(
        body, name="att_bwd",
        out_shape=(jax.ShapeDtypeStruct((3, T, D_ATT), BF16),
                   jax.ShapeDtypeStruct((ATT_GROUPS, WIN_H, HG, KWIN), F32)),
        grid=(ATT_GROUPS,),
        in_specs=[col(0), col(1), col(2), col(0)] + tables + [pl.BlockSpec((GRID_W, GRID_W), lambda g: (0, 0))],
        out_specs=(pl.BlockSpec((3, T, HC), lambda g: (0, 0, g)),
                   pl.BlockSpec((1, WIN_H, HG, KWIN), lambda g: (g, 0, 0, 0))),
        scratch_shapes=[pltpu.VMEM((T, HC), F32), pltpu.VMEM((T, HC), F32), tiles, tiles],
        compiler_params=_params(dimension_semantics=("parallel",)),
    )(qkv, qkv, qkv, datt, bias_rows, jnp.asarray(valid_np) + after, jnp.asarray(same_head_np),
      jnp.asarray(np.eye(GRID_W, dtype=np.float32)[::-1].copy()))


def _conv_taps(up):
    return (_shift_rows(up, 2), _shift_rows(up, 1), up, _shift_rows(up, -1))


def _pair_block_diag(w_pair, dup, same_half):
    return jnp.where(same_half, _dot(w_pair.astype(BF16), dup), 0.0).astype(BF16)


def _gates(u, u16, wa, ba, wi, bi, lam):
    r = _sigmoid(_dot(u16, wa) + ba)
    ig = _sigmoid(_dot(u16, wi) + bi)
    sp = _softplus(-lam)
    log_a = (-LRU_C) * r * sp
    a = jnp.exp(log_a)
    mult2 = jnp.maximum(_one_minus_square(log_a, a), 0.0)
    return r, ig, sp, a, jnp.sqrt(mult2), mult2


SCAN_BLOCKS = 4


def _scans(jobs):
    c = jobs[0][0].shape[1]
    nblk = T // 8
    rows = lax.broadcasted_iota(jnp.int32, (8, c), 0)

    def block(a, b, reverse):
        for s in (1, 2, 4):
            if reverse:
                keep = rows < 8 - s
                a_s = jnp.where(keep, pltpu.roll(a, 8 - s, 0), 1.0)
                b_s = jnp.where(keep, pltpu.roll(b, 8 - s, 0), 0.0)
            else:
                keep = rows >= s
                a_s = jnp.where(keep, pltpu.roll(a, s, 0), 1.0)
                b_s = jnp.where(keep, pltpu.roll(b, s, 0), 0.0)
            b = a * b_s + b
            a = a * a_s
        return a, b

    def step(i, carry):
        out = []
        for (a_ref, b_ref, h_ref, reverse), h_prev in zip(jobs, carry):
            for u in range(SCAN_BLOCKS):
                blk = i * SCAN_BLOCKS + u
                if reverse:
                    blk = nblk - 1 - blk
                t0 = pl.multiple_of(blk * 8, 8)
                a, b = block(a_ref[pl.ds(t0, 8), :], b_ref[pl.ds(t0, 8), :], reverse)
                h = a * h_prev + b
                h_ref[pl.ds(t0, 8), :] = h
                h_prev = jnp.broadcast_to(h[0:1] if reverse else h[7:8], (8, c))
            out.append(h_prev)
        return tuple(out)

    lax.fori_loop(0, nblk // SCAN_BLOCKS, step, tuple(jnp.zeros((8, c), F32) for _ in jobs))


def _rec_specs():
    tok = lambda off: pl.BlockSpec((T, CG), lambda g: (0, g + off))
    per_ch = lambda rows: pl.BlockSpec((rows, CG), lambda g: (0, g))
    wspec = pl.BlockSpec((2, 1, CG, REC_BLOCK), lambda g: (0, g, 0, 0))
    const = lambda shape: pl.BlockSpec(shape, lambda g: (0, 0))
    return tok, per_ch, wspec, const


def _rec_fwd(uy, conv_w, conv_b, w_a, b_a, w_i, b_i, lam):
    tok, per_ch, wspec, const = _rec_specs()

    def body(up_ref, yb_ref, cw_ref, cb_ref, wa_ref, ba_ref, wi_ref, bi_ref, lam_ref, dup_ref, half_ref,
             hf_ref, hb_ref, yrec_ref, am_ref, bx_f, bx_b):
        dup = dup_ref[...]
        same_half = half_ref[...] > 0.5
        taps = _conv_taps(up_ref[...])
        u = cb_ref[...]
        for j in range(4):
            u = u + taps[j] * cw_ref[j:j + 1, :]
        u16 = u.astype(BF16)
        for d, bx_s in enumerate((bx_f, bx_b)):
            wa = _pair_block_diag(wa_ref[d, 0], dup, same_half)
            wi = _pair_block_diag(wi_ref[d, 0], dup, same_half)
            _, ig, _, a, mult, _ = _gates(u, u16, wa, ba_ref[d:d + 1, :], wi, bi_ref[d:d + 1, :],
                                       lam_ref[d:d + 1, :])
            am_ref[2 * d] = a
            am_ref[2 * d + 1] = mult
            bx_s[...] = mult * (ig * u)
        _scans([(am_ref.at[0], bx_f, hf_ref, False), (am_ref.at[2], bx_b, hb_ref, True)])
        gelu, _ = _gelu_and_grad(yb_ref[...])
        yrec_ref[...] = ((hf_ref[...] + hb_ref[...]) * gelu).astype(BF16)

    return pl.pallas_call(
        body, name="rec_fwd",
        out_shape=(jax.ShapeDtypeStruct((T, D_REC), F32), jax.ShapeDtypeStruct((T, D_REC), F32),
                   jax.ShapeDtypeStruct((T, D_REC), BF16), jax.ShapeDtypeStruct((4, T, D_REC), F32)),
        grid=(N_CG,),
        in_specs=[tok(0), tok(N_CG), per_ch(4), per_ch(1), wspec, per_ch(2), wspec, per_ch(2), per_ch(2),
                  const((REC_BLOCK, CG)), const((CG, CG))],
        out_specs=(tok(0), tok(0), tok(0), pl.BlockSpec((4, T, CG), lambda g: (0, 0, g))),
        scratch_shapes=[pltpu.VMEM((T, CG), F32)] * 2,
        compiler_params=_params(dimension_semantics=("parallel",)),
    )(uy, uy, conv_w, conv_b, w_a, b_a, w_i, b_i, lam,
      jnp.asarray(_dup_table(), BF16), jnp.asarray(_pair_mask()))


def _rec_bwd(uy, hf, hb, am, dyrec, conv_w, conv_b, w_a, b_a, w_i, b_i, lam):
    tok, per_ch, wspec, const = _rec_specs()

    def body(up_ref, yb_ref, hf_ref, hb_ref, am_ref, dy_ref, cw_ref, cb_ref, wa_ref, ba_ref, wi_ref, bi_ref,
             lam_ref, dup_ref, dupt_ref, half_ref,
             duy_ref, dcw_ref, dcb_ref, dwa_ref, dba_ref, dwi_ref, dbi_ref, dlam_ref,
             a_s0, a_s1, dh_s, g_s0, g_s1):
        dup = dup_ref[...]
        dup_t = dupt_ref[...]
        same_half = half_ref[...] > 0.5
        taps = _conv_taps(up_ref[...])
        u = cb_ref[...]
        for j in range(4):
            u = u + taps[j] * cw_ref[j:j + 1, :]
        u16 = u.astype(BF16)
        gelu, dgelu = _gelu_and_grad(yb_ref[...])
        dy = dy_ref[...]
        duy_ref[1] = (dy * (hf_ref[...] + hb_ref[...]) * dgelu).astype(BF16)
        dh_s[...] = dy * gelu
        a_s0[...] = _shift_rows(am_ref[0], -1)
        a_s1[...] = _shift_rows(am_ref[2], 1)
        _scans([(a_s0, dh_s, g_s0, True), (a_s1, dh_s, g_s1, False)])
        du = jnp.zeros((T, CG), F32)
        for d, g_s in enumerate((g_s0, g_s1)):
            reverse = d == 1
            wa = _pair_block_diag(wa_ref[d, 0], dup, same_half)
            wi = _pair_block_diag(wi_ref[d, 0], dup, same_half)
            lam_d = lam_ref[d:d + 1, :]
            r = _sigmoid(_dot(u16, wa) + ba_ref[d:d + 1, :])
            ig = _sigmoid(_dot(u16, wi) + bi_ref[d:d + 1, :])
            sp = _softplus(-lam_d)
            a, mult = am_ref[2 * d], am_ref[2 * d + 1]
            mult2 = mult * mult
            g = g_s[...]
            h_prev = _shift_rows(hb_ref[...], -1) if reverse else _shift_rows(hf_ref[...], 1)
            da = g * h_prev
            dmult = g * (ig * u)
            dig = g * mult * u
            du = du + g * mult * ig
            dmult_dlog = jnp.where(mult2 > 0.0, -(a * a) * lax.rsqrt(mult2), 0.0)
            dlog_a = da * a + dmult * dmult_dlog
            dr = dlog_a * ((-LRU_C) * sp)
            dsp = jnp.sum(dlog_a * ((-LRU_C) * r), axis=0, keepdims=True)
            dlam_ref[d:d + 1, :] = dsp * (-_sigmoid(-lam_d))
            dga = dr * r * (1.0 - r)
            dgi = dig * ig * (1.0 - ig)
            dga16 = dga.astype(BF16)
            dgi16 = dgi.astype(BF16)
            du = du + _dot_nt(dga16, wa) + _dot_nt(dgi16, wi)
            dwa_ref[d, 0] = _dot_exact(jnp.where(same_half, _dot_tn(u16, dga16), 0.0), dup_t)
            dwi_ref[d, 0] = _dot_exact(jnp.where(same_half, _dot_tn(u16, dgi16), 0.0), dup_t)
            dba_ref[d:d + 1, :] = jnp.sum(dga, axis=0, keepdims=True)
            dbi_ref[d:d + 1, :] = jnp.sum(dgi, axis=0, keepdims=True)
        dcb_ref[...] = jnp.sum(du, axis=0, keepdims=True)
        for j in range(4):
            dcw_ref[j:j + 1, :] = jnp.sum(du * taps[j], axis=0, keepdims=True)
        dup_in = (_shift_rows(du, -2) * cw_ref[0:1, :] + _shift_rows(du, -1) * cw_ref[1:2, :]
                  + du * cw_ref[2:3, :] + _shift_rows(du, 1) * cw_ref[3:4, :])
        duy_ref[0] = dup_in.astype(BF16)

    wshape = jax.ShapeDtypeStruct((2, N_CG, CG, REC_BLOCK), F32)
    vec = lambda rows: jax.ShapeDtypeStruct((rows, D_REC), F32)
    dup_np = _dup_table()
    return pl.pallas_call(
        body, name="rec_bwd",
        out_shape=(jax.ShapeDtypeStruct((2, T, D_REC), BF16),
                   vec(4), vec(1), wshape, vec(2), wshape, vec(2), vec(2)),
        grid=(N_CG,),
        in_specs=[tok(0), tok(N_CG), tok(0), tok(0), pl.BlockSpec((4, T, CG), lambda g: (0, 0, g)), tok(0),
                  per_ch(4), per_ch(1), wspec, per_ch(2), wspec, per_ch(2), per_ch(2),
                  const((REC_BLOCK, CG)), const((CG, REC_BLOCK)), const((CG, CG))],
        out_specs=(pl.BlockSpec((2, T, CG), lambda g: (0, 0, g)),
                   per_ch(4), per_ch(1), wspec, per_ch(2), wspec, per_ch(2), per_ch(2)),
        scratch_shapes=[pltpu.VMEM((T, CG), F32)] * 5,
        compiler_params=_params(dimension_semantics=("parallel",)),
    )(uy, uy, hf, hb, am, dyrec, conv_w, conv_b, w_a, b_a, w_i, b_i, lam,
      jnp.asarray(dup_np, BF16), jnp.asarray(dup_np.T.copy()), jnp.asarray(_pair_mask()))


TM_MIX = 256


def _mix_specs():
    tok = lambda width, blk=0: pl.BlockSpec((TM_MIX, width), lambda i: (i, blk))
    full = lambda shape: pl.BlockSpec(shape, lambda i: (0, 0))
    return tok, full


def _mix_fwd(x, att, yrec, gg, w_att_o_t, w_rec_o, w_out):
    tok, full = _mix_specs()

    def body(x_ref, att_ref, yr_ref, ga_ref, gr_ref, wao_ref, wro_ref, wo_ref, x1_ref, mixed_ref):
        y_att = _dot_nt(att_ref[...], wao_ref[...])
        y_rec = _dot(yr_ref[...], wro_ref[...])
        mixed = (_sigmoid(ga_ref[...]) * y_att + _sigmoid(gr_ref[...]) * y_rec).astype(BF16)
        mixed_ref[...] = mixed
        x1_ref[...] = x_ref[...] + _dot(mixed, wo_ref[...])

    return pl.pallas_call(
        body, name="mix_fwd",
        out_shape=(jax.ShapeDtypeStruct((T, D), F32), jax.ShapeDtypeStruct((T, D), BF16)),
        grid=(T // TM_MIX,),
        in_specs=[tok(D), tok(D_ATT), tok(D_REC), tok(D, 0), tok(D, 1),
                  full((D, D_ATT)), full((D_REC, D)), full((D, D))],
        out_specs=(tok(D), tok(D)),
        compiler_params=_params(dimension_semantics=("parallel",)),
    )(x, att, yrec, gg, gg, w_att_o_t, w_rec_o, w_out)


def _mix_bwd(dx1, att, yrec, gg, w_att_o_t, w_rec_o, w_out):
    tok, full = _mix_specs()

    def body(dx_ref, att_ref, yr_ref, ga_ref, gr_ref, wao_ref, wro_ref, wo_ref,
             dgg_ref, dya_ref, dyr_ref, datt_ref, dyrp_ref):
        dmixed = _dot_nt(dx_ref[...].astype(BF16), wo_ref[...])
        y_att = _dot_nt(att_ref[...], wao_ref[...])
        y_rec = _dot(yr_ref[...], wro_ref[...])
        sa = _sigmoid(ga_ref[...])
        sr = _sigmoid(gr_ref[...])
        dgg_ref[0] = (dmixed * y_att * sa * (1.0 - sa)).astype(BF16)
        dgg_ref[1] = (dmixed * y_rec * sr * (1.0 - sr)).astype(BF16)
        dya = (dmixed * sa).astype(BF16)
        dyr = (dmixed * sr).astype(BF16)
        dya_ref[...] = dya
        dyr_ref[...] = dyr
        datt_ref[...] = _dot(dya, wao_ref[...]).astype(BF16)
        dyrp_ref[...] = _dot_nt(dyr, wro_ref[...])

    return pl.pallas_call(
        body, name="mix_bwd",
        out_shape=(jax.ShapeDtypeStruct((2, T, D), BF16),
                   jax.ShapeDtypeStruct((T, D), BF16), jax.ShapeDtypeStruct((T, D), BF16),
                   jax.ShapeDtypeStruct((T, D_ATT), BF16), jax.ShapeDtypeStruct((T, D_REC), F32)),
        grid=(T // TM_MIX,),
        in_specs=[tok(D), tok(D_ATT), tok(D_REC), tok(D, 0), tok(D, 1),
                  full((D, D_ATT)), full((D_REC, D)), full((D, D))],
        out_specs=(pl.BlockSpec((2, TM_MIX, D), lambda i: (0, i, 0)),
                   tok(D), tok(D), tok(D_ATT), tok(D_REC)),
        compiler_params=_params(dimension_semantics=("parallel",)),
    )(dx1, att, yrec, gg, gg, w_att_o_t, w_rec_o, w_out)


TM_FFN = 256
FF_CHUNK = 1024


def _ffn_loss(x1, target, g2, gf, w_ff1_t, w_ff2):
    n_chunks = D_FF // FF_CHUNK

    def body(x1_ref, tg_ref, g2_ref, gf_ref, w1_hbm, w2_hbm,
             loss_ref, dx1_ref, h2_ref, act_ref, dpre_ref, dx2_ref, dg2_ref, dgf_ref,
             w1, w2, relu_s):
        i = pl.program_id(0)

        @pl.when(i == 0)
        def _():
            pltpu.sync_copy(w1_hbm, w1)
            pltpu.sync_copy(w2_hbm, w2)
            loss_ref[...] = jnp.zeros_like(loss_ref)
            dg2_ref[...] = jnp.zeros_like(dg2_ref)
            dgf_ref[...] = jnp.zeros_like(dgf_ref)

        x1v = x1_ref[...]
        r2 = lax.rsqrt(jnp.mean(x1v * x1v, axis=-1, keepdims=True) + EPS)
        xh2 = x1v * r2
        h2 = (xh2 * g2_ref[...]).astype(BF16)
        h2_ref[...] = h2
        x2 = x1v
        for c in range(n_chunks):
            ff = slice(c * FF_CHUNK, (c + 1) * FF_CHUNK)
            rl = jnp.maximum(_dot_nt(h2, w1[ff, :]), 0.0)
            relu_s[:, ff] = rl
            act = (rl * rl).astype(BF16)
            act_ref[:, ff] = act
            x2 = x2 + _dot(act, w2[ff, :])
        r3 = lax.rsqrt(jnp.mean(x2 * x2, axis=-1, keepdims=True) + EPS)
        xh3 = x2 * r3
        err = xh3 * gf_ref[...] - tg_ref[...]
        loss_ref[...] += 0.5 * jnp.sum(jnp.mean(err * err, axis=-1, keepdims=True))
        dy = err * (1.0 / D)
        dgf_ref[...] += jnp.sum(dy * xh3, axis=0, keepdims=True)
        dx2 = _rms_bwd(dy, xh3, r3, gf_ref[...])
        dx2_16 = dx2.astype(BF16)
        dx2_ref[...] = dx2_16
        dh2 = jnp.zeros((TM_FFN, D), F32)
        for c in range(n_chunks):
            ff = slice(c * FF_CHUNK, (c + 1) * FF_CHUNK)
            dpre = (_dot_nt(dx2_16, w2[ff, :]) * (2.0 * relu_s[:, ff])).astype(BF16)
            dpre_ref[:, ff] = dpre
            dh2 = dh2 + _dot(dpre, w1[ff, :])
        dg2_ref[...] += jnp.sum(dh2 * xh2, axis=0, keepdims=True)
        dx1_ref[...] = dx2 + _rms_bwd(dh2, xh2, r2, g2_ref[...])

    tok = lambda width: pl.BlockSpec((TM_FFN, width), lambda i: (i, 0))
    vec = pl.BlockSpec((1, D), lambda i: (0, 0))
    hbm = pl.BlockSpec(memory_space=pl.ANY)
    return pl.pallas_call(
        body, name="ffn_loss",
        out_shape=(jax.ShapeDtypeStruct((8, 128), F32), jax.ShapeDtypeStruct((T, D), F32),
                   jax.ShapeDtypeStruct((T, D), BF16), jax.ShapeDtypeStruct((T, D_FF), BF16),
                   jax.ShapeDtypeStruct((T, D_FF), BF16), jax.ShapeDtypeStruct((T, D), BF16),
                   jax.ShapeDtypeStruct((1, D), F32), jax.ShapeDtypeStruct((1, D), F32)),
        grid=(T // TM_FFN,),
        in_specs=[tok(D), tok(D), vec, vec, hbm, hbm],
        out_specs=(pl.BlockSpec((8, 128), lambda i: (0, 0)), tok(D), tok(D), tok(D_FF), tok(D_FF), tok(D),
                   vec, vec),
        scratch_shapes=[pltpu.VMEM((D_FF, D), BF16), pltpu.VMEM((D_FF, D), BF16),
                        pltpu.VMEM((TM_FFN, D_FF), F32)],
        compiler_params=_params(dimension_semantics=("arbitrary",)),
    )(x1, target, g2, gf, w_ff1_t, w_ff2)


def _local_step(x, target, p, late_weights, reduce_early):
    bias = _rpb_rows(p["rpb"])
    pairs = lambda w: w.reshape(2, N_CG, CG, REC_BLOCK)
    w_a, w_i = pairs(p["w_rg_a"]), pairs(p["w_rg_i"])
    rec_params = (p["conv_w"], p["conv_b"], w_a, p["b_rg_a"], w_i, p["b_rg_i"], p["lru_lambda"])

    qkv, uy, gg, h = _in_proj(x, p["ln1_g"], p["w_in_t"], p["b_in"])
    att = _att_fwd(qkv, bias)
    hf, hb, yrec, am = _rec_fwd(uy, *rec_params)
    p = {**p, **late_weights(yrec, 0)}
    x1, mixed = _mix_fwd(x, att, yrec, gg, p["w_att_o_t"], p["w_rec_o"], p["w_out"])
    p = {**p, **late_weights(x1, 1)}
    loss8, dx1, h2, act, dpre, dx2, g_ln2, g_lnf = _ffn_loss(
        x1, target, p["ln2_g"], p["lnf_g"], p["w_ff1_t"], p["w_ff2"])

    dgg, dya, dyr, datt, dyrp = _mix_bwd(dx1, att, yrec, gg, p["w_att_o_t"], p["w_rec_o"], p["w_out"])
    duy, g_cw, g_cb, g_wa, g_ba, g_wi, g_bi, g_lam = _rec_bwd(uy, hf, hb, am, dyrp, *rec_params)
    blocks = lambda g: g.reshape(2, N_REC_BLOCKS, REC_BLOCK, REC_BLOCK)
    grads = {
        "w_att_o_t": _matmul(dya, att, "tn", BF16, "g_w_att_o"),
        "conv_w": g_cw, "conv_b": g_cb, "w_rg_a": blocks(g_wa), "b_rg_a": g_ba,
        "w_rg_i": blocks(g_wi), "b_rg_i": g_bi, "lru_lambda": g_lam,
        "w_rec_o": _matmul(yrec, dyr, "tn", BF16, "g_w_rec_o"),
        "w_out": _matmul(mixed, dx1, "tn", BF16, "g_w_out"),
        "ln2_g": g_ln2,
        "w_ff1_t": _matmul(dpre, h2, "tn", BF16, "g_w_ff1"),
        "w_ff2": _matmul(act, dx2, "tn", BF16, "g_w_ff2"),
        "lnf_g": g_lnf,
    }
    dqkv, gbias = _att_bwd(qkv, bias, datt, reduce_early(grads))
    dz = (dqkv, duy, dgg)
    grad_x, g_ln1 = _dh_norm1_bwd(dz, p["w_in_t"], x, p["ln1_g"], dx1)
    g_w_in_t, g_b_in = _grad_w_in(dz, h)
    grads.update(ln1_g=g_ln1, w_in_t=g_w_in_t, b_in=g_b_in, rpb=_rpb_fold(gbias))
    return loss8[0:1, 0:1], grad_x, grads


MESH_ID = pl.DeviceIdType.MESH
ANY = pl.BlockSpec(memory_space=pl.ANY)

CHAN_BLOCK_ROWS = 32
GATE_ROWS = 2 * 2 * N_REC_BLOCKS * REC_BLOCK * REC_BLOCK // (N_DEV * D)
SECTIONS = (("w_in_t", 704, D), ("w_rec_o", 128, D), ("w_out", 128, D), ("w_ff1_t", 512, D),
            ("w_ff2", 512, D), ("chan", CHAN_BLOCK_ROWS, D), ("w_att_o_t", 128, D_ATT),
            ("gates", GATE_ROWS, D))
N_SEC = len(SECTIONS)
N_CHAN_ROWS = 10
CHAN = (("conv_w", 4), ("b_rg_a", 2), ("b_rg_i", 2), ("lru_lambda", 2))


def _position():
    return lax.axis_index("x"), lax.axis_index("y"), lax.axis_index("c")


def _other_chips(x, y):
    return [(1 - x, y), (x, 1 - y), (1 - x, 1 - y)]


PASS_ON_IDS, PAIR_EARLY_ID, PAIR_LATE_ID = (1, 4), 2, 3


def _pair_handshake(x, y, c):
    barrier = pltpu.get_barrier_semaphore()
    pl.semaphore_signal(barrier, inc=1, device_id=(x, y, 1 - c), device_id_type=MESH_ID)
    pl.semaphore_wait(barrier, 1)


def _block_of(ref, dev, rows):
    return ref.at[pl.ds(pl.multiple_of(dev * rows, 16), rows)]


def _all_gather(shards, name):
    ns = len(shards)

    def body(*refs):
        x_refs, out_refs, done_ref = refs[:ns], refs[ns:2 * ns], refs[2 * ns]
        send_sems, recv_sems, local_sems = refs[2 * ns + 1:]
        done_ref[0, 0] = 0.0
        x, y, c = _position()
        me, sibling = (x, y, c), (x, y, 1 - c)
        x_nbr, y_nbr, diagonal = _other_chips(x, y)
        north = c == 1
        relay_from = (jnp.where(north, x_nbr[0], y_nbr[0]), jnp.where(north, x_nbr[1], y_nbr[1]))
        relay_to = (jnp.where(north, y_nbr[0], x_nbr[0]), jnp.where(north, y_nbr[1], x_nbr[1]))

        def rows(s, px, py, pc):
            return _block_of(out_refs[s], 4 * px + 2 * py + pc, shards[s].shape[0])

        def copy(k, s, block, to, from_shard=False):
            return pltpu.make_async_remote_copy(
                src_ref=x_refs[s] if from_shard else rows(s, *block), dst_ref=rows(s, *block),
                send_sem=send_sems.at[k * ns + s], recv_sem=recv_sems.at[k * ns + s],
                device_id=to, device_id_type=MESH_ID)

        sections = range(ns)
        mine = [pltpu.make_async_copy(x_refs[s], rows(s, *me), local_sems.at[s]) for s in sections]
        sent = [copy(k, s, me, to, True) for k, to in enumerate((sibling, (*x_nbr, c), (*y_nbr, c)))
                for s in sections]
        for cp in mine + sent:
            cp.start()
        for s in sections:
            copy(1, s, (*x_nbr, c), me).wait_recv()
            copy(2, s, (*y_nbr, c), me).wait_recv()
            sent += [copy(3, s, (*relay_from, c), (*relay_to, c)),
                     copy(4, s, (*x_nbr, c), sibling), copy(5, s, (*y_nbr, c), sibling)]
            for cp in sent[-3:]:
                cp.start()
        for s in sections:
            copy(3, s, (*diagonal, c), me).wait_recv()
            sent.append(copy(6, s, (*diagonal, c), sibling))
            sent[-1].start()
        for s in sections:
            copy(0, s, sibling, me).wait_recv()
            for k, chip in ((4, x_nbr), (5, y_nbr), (6, diagonal)):
                copy(k, s, (*chip, 1 - c), me).wait_recv()
        for cp in sent:
            cp.wait_send()
        for cp in mine:
            cp.wait()

    return pl.pallas_call(
        body, name=name,
        out_shape=tuple(jax.ShapeDtypeStruct((N_DEV * s.shape[0], s.shape[1]), s.dtype) for s in shards)
        + (jax.ShapeDtypeStruct((1, 1), F32),),
        in_specs=[ANY] * ns,
        out_specs=(ANY,) * ns + (pl.BlockSpec(memory_space=pltpu.SMEM),),
        scratch_shapes=[pltpu.SemaphoreType.DMA((7 * ns,)), pltpu.SemaphoreType.DMA((7 * ns,)),
                        pltpu.SemaphoreType.DMA((ns,))],
    )(*shards)


HBM = pl.BlockSpec(memory_space=pltpu.HBM)
SEM = pl.BlockSpec(memory_space=pltpu.SEMAPHORE)
EFFECT = pltpu.SideEffectType.DATAFLOW_SIDE_EFFECTING


def _in_hbm(a):
    return pltpu.with_memory_space_constraint(a, pltpu.HBM)


def _first_hop_copies(shards, x_refs, zones, send_sems, recv_sems):
    ns = len(shards)
    x, y, c = _position()
    targets = [(x, y, 1 - c)] + [(cx, cy, c) for cx, cy in _other_chips(x, y)]
    return [pltpu.make_async_remote_copy(
        src_ref=x_refs[s], dst_ref=_block_of(zones[s], 4 * x + 2 * y + c, shards[s].shape[0]),
        send_sem=send_sems.at[k * ns + s], recv_sem=recv_sems.at[k * ns + s],
        device_id=to, device_id_type=MESH_ID)
        for k, to in enumerate(targets) for s in range(ns)]


def _after_all(arrays, name):
    def body(*refs):
        refs[-1][...] = jnp.zeros_like(refs[-1])

    return pl.pallas_call(
        body, name=name,
        out_shape=jax.ShapeDtypeStruct((8, LANES), F32),
        in_specs=[pl.BlockSpec(memory_space=pl.ANY)] * len(arrays),
        out_specs=pl.BlockSpec(memory_space=pltpu.VMEM),
    )(*arrays)


def _own_blocks_placed(shards, after):
    ns = len(shards)
    x, y, c = _position()
    me = jnp.reshape(4 * x + 2 * y + c, (1,)).astype(jnp.int32)
    shards = [*shards[:-1], shards[-1] + after.astype(shards[-1].dtype)]

    def body(me_ref, *refs):
        for s in range(ns):
            refs[ns + s][...] = refs[s][...]

    return pl.pallas_call(
        body, name="own_blocks_placed",
        out_shape=tuple(jax.ShapeDtypeStruct((N_DEV * s.shape[0], s.shape[1]), s.dtype) for s in shards),
        grid_spec=pltpu.PrefetchScalarGridSpec(
            num_scalar_prefetch=1, grid=(1,),
            in_specs=[pl.BlockSpec(s.shape, lambda i, me: (0, 0)) for s in shards],
            out_specs=tuple(pl.BlockSpec(s.shape, lambda i, me: (me[0], 0)) for s in shards)),
        compiler_params=_params(dimension_semantics=("arbitrary",)),
    )(me, *shards)


def _gather_start(shards, after, name):
    ns = len(shards)
    zones = _own_blocks_placed(shards, after)

    def body(*refs):
        for cp in _first_hop_copies(shards, refs[:ns], refs[ns:2 * ns], refs[2 * ns], refs[2 * ns + 1]):
            cp.start()
        refs[-1][...] = jnp.zeros_like(refs[-1])

    out = pl.pallas_call(
        body, name=name,
        out_shape=(pltpu.SemaphoreType.DMA((4 * ns,)), pltpu.SemaphoreType.DMA((4 * ns,)),
                   *[pltpu.HBM(a.shape, a.dtype) for a in (*shards, *zones)],
                   jax.ShapeDtypeStruct((8, LANES), F32)),
        in_specs=[HBM] * (2 * ns),
        out_specs=(SEM, SEM, *[HBM] * (2 * ns), pl.BlockSpec(memory_space=pltpu.VMEM)),
        input_output_aliases={i: 2 + i for i in range(2 * ns)},
        compiler_params=pltpu.CompilerParams(has_side_effects=EFFECT),
    )(*[_in_hbm(a) for a in shards], *[_in_hbm(a) for a in zones])
    return out[0], out[1], out[2:2 + ns], out[2 + ns:2 + 2 * ns], out[-1]


def _gather_wait(send_sems, recv_sems, shards, zones, which, after, name):
    ns = len(shards)

    def body(*refs):
        copies = _first_hop_copies(shards, refs[:ns], refs[ns:2 * ns], refs[2 * ns], refs[2 * ns + 1])
        for i, cp in enumerate(copies):
            if i % ns in which:
                cp.wait_send()
                cp.wait_recv()

    out = pl.pallas_call(
        body, name=name,
        out_shape=tuple(pltpu.HBM(a.shape, a.dtype) for a in (*shards, *zones)),
        in_specs=[HBM] * (2 * ns) + [SEM, SEM, ANY],
        out_specs=(HBM,) * (2 * ns),
        input_output_aliases={i: i for i in range(2 * ns)},
        compiler_params=pltpu.CompilerParams(has_side_effects=EFFECT),
    )(*shards, *zones, send_sems, recv_sems, after)
    return out[:ns], out[ns:]


def _gather_pass_on(rows, zones, barrier_id, name):
    ns = len(zones)

    def body(*refs):
        in_refs, out_refs = refs[:ns], refs[ns:2 * ns]
        send_sems, recv_sems = refs[2 * ns:]
        x, y, c = _position()
        _pair_handshake(x, y, c)
        copies = [pltpu.make_async_remote_copy(
            src_ref=_block_of(in_refs[s], 4 * cx + 2 * cy + c, rows[s]),
            dst_ref=_block_of(out_refs[s], 4 * cx + 2 * cy + c, rows[s]),
            send_sem=send_sems.at[j * ns + s], recv_sem=recv_sems.at[j * ns + s],
            device_id=(x, y, 1 - c), device_id_type=MESH_ID)
            for j, (cx, cy) in enumerate(_other_chips(x, y)) for s in range(ns)]
        for cp in copies:
            cp.start()
        for cp in copies:
            cp.wait_recv()
        for cp in copies:
            cp.wait_send()

    return pl.pallas_call(
        body, name=name,
        out_shape=tuple(jax.ShapeDtypeStruct(z.shape, z.dtype) for z in zones),
        in_specs=[ANY] * ns, out_specs=(ANY,) * ns,
        input_output_aliases={i: i for i in range(ns)},
        scratch_shapes=[pltpu.SemaphoreType.DMA((3 * ns,)), pltpu.SemaphoreType.DMA((3 * ns,))],
        compiler_params=pltpu.CompilerParams(collective_id=barrier_id),
    )(*zones)


def _pair_copies(sections, g_refs, land, send_sems, recv_sems):
    ns = len(sections)
    x, y, c = _position()
    return [pltpu.make_async_remote_copy(
        src_ref=_block_of(g_refs[s], 2 * k + 1 - c, rows), dst_ref=land[s].at[k],
        send_sem=send_sems.at[k * ns + s], recv_sem=recv_sems.at[k * ns + s],
        device_id=(x, y, 1 - c), device_id_type=MESH_ID)
        for k in range(N_CHIPS) for s, (_, rows, _) in enumerate(sections)]


def _pair_exchange_start(sections, grads, barrier_id, name):
    ns = len(sections)

    def body(*refs):
        _pair_handshake(*_position())
        for cp in _pair_copies(sections, refs[:ns], refs[ns:2 * ns], refs[2 * ns], refs[2 * ns + 1]):
            cp.start()
        refs[-1][...] = jnp.zeros_like(refs[-1])

    zones = [lax.empty((N_CHIPS, rows, cols), BF16) for _, rows, cols in sections]
    n = N_CHIPS * ns
    out = pl.pallas_call(
        body, name=name,
        out_shape=(pltpu.SemaphoreType.DMA((n,)), pltpu.SemaphoreType.DMA((n,)),
                   *[pltpu.HBM(a.shape, a.dtype) for a in (*grads, *zones)],
                   jax.ShapeDtypeStruct((8, LANES), F32)),
        in_specs=[HBM] * (2 * ns),
        out_specs=(SEM, SEM, *[HBM] * (2 * ns), pl.BlockSpec(memory_space=pltpu.VMEM)),
        input_output_aliases={i: 2 + i for i in range(2 * ns)},
        compiler_params=pltpu.CompilerParams(has_side_effects=EFFECT, collective_id=barrier_id),
    )(*[_in_hbm(a) for a in grads], *[_in_hbm(a) for a in zones])
    return out[0], out[1], out[2:2 + ns], out[2 + ns:2 + 2 * ns], out[-1]


def _pair_exchange_wait(sections, send_sems, recv_sems, grads, zones, after, name):
    ns = len(sections)

    def body(*refs):
        for cp in _pair_copies(sections, refs[:ns], refs[ns:2 * ns], refs[2 * ns], refs[2 * ns + 1]):
            cp.wait_send()
            cp.wait_recv()

    out = pl.pallas_call(
        body, name=name,
        out_shape=tuple(pltpu.HBM(a.shape, a.dtype) for a in (*grads, *zones)),
        in_specs=[HBM] * (2 * ns) + [SEM, SEM, ANY],
        out_specs=(HBM,) * (2 * ns),
        input_output_aliases={i: i for i in range(2 * ns)},
        compiler_params=pltpu.CompilerParams(has_side_effects=EFFECT),
    )(*grads, *zones, send_sems, recv_sems, after)
    return out[:ns], out[ns:]


def _pair_add(sections, grads, got, core, name):
    ns = len(sections)

    def body(core_ref, *refs):
        g_refs, got_refs, p_refs = refs[:ns], refs[ns:2 * ns], refs[2 * ns:]
        for s in range(ns):
            p_refs[s][0] = (g_refs[s][...].astype(F32) + got_refs[s][0].astype(F32)).astype(BF16)

    slot = [pl.BlockSpec((1, rows, cols), lambda k, c: (k, 0, 0)) for _, rows, cols in sections]
    return pl.pallas_call(
        body, name=name,
        out_shape=tuple(jax.ShapeDtypeStruct((N_CHIPS, rows, cols), BF16) for _, rows, cols in sections),
        grid_spec=pltpu.PrefetchScalarGridSpec(
            num_scalar_prefetch=1, grid=(N_CHIPS,),
            in_specs=[pl.BlockSpec((rows, cols), lambda k, c: (2 * k + c[0], 0)) for _, rows, cols in sections]
            + slot,
            out_specs=tuple(slot)),
        compiler_params=_params(dimension_semantics=("parallel",)),
    )(core, *grads, *got)


def _chip_copies(sections, p_refs, land, send_sems, recv_sems):
    ns = len(sections)
    x, y, c = _position()
    return [pltpu.make_async_remote_copy(
        src_ref=p_refs[s].at[2 * cx + cy], dst_ref=land[s].at[j],
        send_sem=send_sems.at[j * ns + s], recv_sem=recv_sems.at[j * ns + s],
        device_id=(cx, cy, c), device_id_type=MESH_ID)
        for j, (cx, cy) in enumerate(_other_chips(x, y)) for s in range(ns)]


def _chip_exchange(sections, parts, name):
    ns = len(sections)

    def body(*refs):
        copies = _chip_copies(sections, refs[:ns], refs[ns:2 * ns], *refs[2 * ns:])
        for cp in copies:
            cp.start()
        for cp in copies:
            cp.wait_recv()
        for cp in copies:
            cp.wait_send()

    n = 3 * ns
    return pl.pallas_call(
        body, name=name,
        out_shape=tuple(jax.ShapeDtypeStruct((3, rows, cols), BF16) for _, rows, cols in sections),
        in_specs=[ANY] * ns, out_specs=(ANY,) * ns,
        scratch_shapes=[pltpu.SemaphoreType.DMA((n,)), pltpu.SemaphoreType.DMA((n,))],
    )(*parts)


def _chip_exchange_start(sections, parts, name):
    ns = len(sections)

    def body(*refs):
        p_refs, land = refs[:ns], refs[ns:2 * ns]
        send_sems, recv_sems = refs[2 * ns], refs[2 * ns + 1]
        token = refs[-1]
        for cp in _chip_copies(sections, p_refs, land, send_sems, recv_sems):
            cp.start()
        token[...] = jnp.zeros_like(token)

    zones = [lax.empty((3, rows, cols), BF16) for _, rows, cols in sections]
    out = pl.pallas_call(
        body, name=name,
        out_shape=(pltpu.SemaphoreType.DMA((3 * ns,)), pltpu.SemaphoreType.DMA((3 * ns,)),
                   *[pltpu.HBM(a.shape, a.dtype) for a in parts], *[pltpu.HBM(a.shape, a.dtype) for a in zones],
                   jax.ShapeDtypeStruct((8, LANES), F32)),
        in_specs=[HBM] * (2 * ns),
        out_specs=(SEM, SEM, *[HBM] * (2 * ns), pl.BlockSpec(memory_space=pltpu.VMEM)),
        input_output_aliases={i: 2 + i for i in range(2 * ns)},
        compiler_params=pltpu.CompilerParams(has_side_effects=EFFECT),
    )(*[_in_hbm(a) for a in parts], *[_in_hbm(a) for a in zones])
    return out[0], out[1], out[2:2 + ns], out[2 + ns:2 + 2 * ns], out[-1]


def _chip_exchange_wait(sections, send_sems, recv_sems, parts, zones, after, name):
    ns = len(sections)

    def body(*refs):
        p_refs, land = refs[:ns], refs[ns:2 * ns]
        for cp in _chip_copies(sections, p_refs, land, refs[2 * ns], refs[2 * ns + 1]):
            cp.wait_send()
            cp.wait_recv()

    out = pl.pallas_call(
        body, name=name,
        out_shape=tuple(pltpu.HBM(a.shape, a.dtype) for a in (*parts, *zones)),
        in_specs=[HBM] * (2 * ns) + [SEM, SEM, ANY],
        out_specs=(HBM,) * (2 * ns),
        input_output_aliases={i: i for i in range(2 * ns)},
        compiler_params=pltpu.CompilerParams(has_side_effects=EFFECT),
    )(*parts, *zones, send_sems, recv_sems, after)
    return out[:ns], out[ns:]


def _grad_finish(sections, parts, far, chip, name):
    ns = len(sections)

    def body(chip_ref, *refs):
        p_refs, b_refs, g_refs = refs[:ns], refs[ns:2 * ns], refs[2 * ns:]
        for s in range(ns):
            g = p_refs[s][0].astype(F32)
            for j in range(3):
                g = g + b_refs[s][j].astype(F32)
            g_refs[s][...] = g

    half = [(rows // 2, cols) for _, rows, cols in sections]
    return pl.pallas_call(
        body, name=name,
        out_shape=tuple(jax.ShapeDtypeStruct((rows, cols), F32) for _, rows, cols in sections),
        grid_spec=pltpu.PrefetchScalarGridSpec(
            num_scalar_prefetch=1, grid=(2,),
            in_specs=[pl.BlockSpec((1, r, c), lambda i, chip: (chip[0], i, 0)) for r, c in half]
            + [pl.BlockSpec((3, r, c), lambda i, chip: (0, i, 0)) for r, c in half],
            out_specs=tuple(pl.BlockSpec((r, c), lambda i, chip: (i, 0)) for r, c in half)),
        compiler_params=_params(dimension_semantics=("parallel",)),
    )(chip, *parts, *far)


def _sum_devices(parts, rows, name):
    cols = parts.shape[1]
    tr = rows // 2

    def body(*refs):
        s = refs[0][...].astype(F32)
        for d in range(1, N_DEV):
            s = s + refs[d][...].astype(F32)
        refs[N_DEV][...] = s

    return pl.pallas_call(
        body, name=name,
        out_shape=jax.ShapeDtypeStruct((rows, cols), F32),
        grid=(2,),
        in_specs=[pl.BlockSpec((tr, cols), lambda i, d=d: (2 * d + i, 0)) for d in range(N_DEV)],
        out_specs=pl.BlockSpec((tr, cols), lambda i: (i, 0)),
        compiler_params=_params(dimension_semantics=("parallel",)),
    )(*([parts] * N_DEV))


def _adamw_step(w_ref, g_ref, m_ref, v_ref, d_ref, nm_ref, nv_ref):
    c1 = 1.0 / (1.0 - ADAM_B1 ** ADAM_STEP)
    c2 = 1.0 / (1.0 - ADAM_B2 ** ADAM_STEP)
    gv = g_ref[...]
    nm = ADAM_B1 * m_ref[...] + (1.0 - ADAM_B1) * gv
    nv = ADAM_B2 * v_ref[...] + (1.0 - ADAM_B2) * (gv * gv)
    nm_ref[...] = nm
    nv_ref[...] = nv
    d_ref[...] = (-ADAM_LR) * ((nm * c1) / (jnp.sqrt(nv * c2) + ADAM_EPS) + ADAM_WD * w_ref[...])


def _adamw_small(params, name):
    n = len(params)

    def body(*refs):
        for k in range(n):
            _adamw_step(*refs[4 * k:4 * k + 4], *refs[4 * n + 3 * k:4 * n + 3 * k + 3])

    out = pl.pallas_call(
        body, name=name,
        out_shape=tuple(jax.ShapeDtypeStruct(p[0].shape, F32) for p in params for _ in range(3)),
    )(*[a for p in params for a in p])
    return [out[3 * k:3 * k + 3] for k in range(n)]


def _adamw(w, g, m, v, name):
    rows, cols = w.shape
    tr = rows
    while tr * cols * 4 > (1 << 20) and tr % 16 == 0:
        tr //= 2

    def body(*refs):
        _adamw_step(*refs)

    spec = pl.BlockSpec((tr, cols), lambda i: (i, 0))
    shape = jax.ShapeDtypeStruct((rows, cols), F32)
    return pl.pallas_call(
        body, name=name,
        out_shape=(shape, shape, shape),
        grid=(rows // tr,),
        in_specs=[spec] * 4, out_specs=(spec,) * 3,
        compiler_params=_params(dimension_semantics=("parallel",)),
    )(w, g, m, v)


NAMES = ("ln1_g", "w_in", "b_in", "rpb", "w_att_o", "conv_w", "conv_b", "w_rg_a", "b_rg_a", "w_rg_i",
         "b_rg_i", "lru_lambda", "w_rec_o", "w_out", "ln2_g", "w_ff1", "w_ff2", "lnf_g")
TRANSPOSED = {"w_in": "w_in_t", "w_att_o": "w_att_o_t", "w_ff1": "w_ff1_t"}
ROW_SHARDED = ("w_rec_o", "w_out", "w_ff2")
REPLICATED = (("ln1_g", (1, D)), ("b_in", (1, D_IN)), ("rpb", (N_HEADS * N_RPB_R, N_RPB_C)),
              ("conv_b", (1, D_REC)), ("w_rg_a", (2 * N_REC_BLOCKS * REC_BLOCK, REC_BLOCK)),
              ("w_rg_i", (2 * N_REC_BLOCKS * REC_BLOCK, REC_BLOCK)), ("ln2_g", (1, D)), ("lnf_g", (1, D)))
GATE_BLOCKS = ("w_rg_a", "w_rg_i")
SMALL_ROWS = 112


def _chan_bits(vectors):
    chan = jnp.concatenate(vectors, axis=0)
    bits = lax.bitcast_convert_type(chan, BF16).reshape(-1)
    return jnp.pad(bits, (0, CHAN_BLOCK_ROWS * D - bits.shape[0])).reshape(CHAN_BLOCK_ROWS, D)


def _chan_from_bits(gathered):
    bits = gathered.reshape(N_DEV, CHAN_BLOCK_ROWS * D)[:, :2 * N_CHAN_ROWS * LANES]
    chan = lax.bitcast_convert_type(bits.reshape(N_DEV, N_CHAN_ROWS, LANES, 2), F32)
    return chan.transpose(1, 0, 2).reshape(N_CHAN_ROWS, D)


def kernel(x, ln1_g, w_in, b_in, rpb, w_att_o, conv_w, conv_b, w_rg_a, b_rg_a, w_rg_i, b_rg_i, lru_lambda, w_rec_o, w_out, ln2_g, w_ff1, w_ff2, lnf_g, loss_target, m_ln1_g, m_w_in, m_b_in, m_rpb, m_w_att_o, m_conv_w, m_conv_b, m_w_rg_a, m_b_rg_a, m_w_rg_i, m_b_rg_i, m_lru_lambda, m_w_rec_o, m_w_out, m_ln2_g, m_w_ff1, m_w_ff2, m_lnf_g, v_ln1_g, v_w_in, v_b_in, v_rpb, v_w_att_o, v_conv_w, v_conv_b, v_w_rg_a, v_b_rg_a, v_w_rg_i, v_b_rg_i, v_lru_lambda, v_w_rec_o, v_w_out, v_ln2_g, v_w_ff1, v_w_ff2, v_lnf_g):
    w = dict(zip(NAMES, (ln1_g, w_in, b_in, rpb, w_att_o, conv_w, conv_b, w_rg_a, b_rg_a, w_rg_i,
                         b_rg_i, lru_lambda, w_rec_o, w_out, ln2_g, w_ff1, w_ff2, lnf_g)))
    m = dict(zip(NAMES, (m_ln1_g, m_w_in, m_b_in, m_rpb, m_w_att_o, m_conv_w, m_conv_b, m_w_rg_a,
                         m_b_rg_a, m_w_rg_i, m_b_rg_i, m_lru_lambda, m_w_rec_o, m_w_out, m_ln2_g,
                         m_w_ff1, m_w_ff2, m_lnf_g)))
    v = dict(zip(NAMES, (v_ln1_g, v_w_in, v_b_in, v_rpb, v_w_att_o, v_conv_w, v_conv_b, v_w_rg_a,
                         v_b_rg_a, v_w_rg_i, v_b_rg_i, v_lru_lambda, v_w_rec_o, v_w_out, v_ln2_g,
                         v_w_ff1, v_w_ff2, v_lnf_g)))
    xi, yi, ci = _position()

    shard = {t: w[n][0].T.astype(BF16) for n, t in TRANSPOSED.items()}
    shard.update({n: w[n][0].astype(BF16) for n in ROW_SHARDED})
    shard["chan"] = _chan_bits([w[n][0] for n, _ in CHAN])
    first, later = ("w_in_t", "chan"), ("w_rec_o", "w_out", "w_att_o_t", "w_ff1_t", "w_ff2")
    *gathered, done = _all_gather([shard[n] for n in first], "weight_all_gather")
    p = dict(zip(first, gathered))
    send_sems, recv_sems, sent, zones, token = _gather_start([shard[n] for n in later], done,
                                                             "weight_gather_start")

    travelling = {"shards": sent, "zones": zones}
    stages = (("w_rec_o", "w_out", "w_att_o_t"), ("w_ff1_t", "w_ff2"))

    def late_weights(after, stage):
        which = [later.index(n) for n in stages[stage]]
        travelling["shards"], travelling["zones"] = _gather_wait(
            send_sems, recv_sems, travelling["shards"], travelling["zones"], which, after,
            "weight_gather_wait_%d" % stage)
        return dict(zip(stages[stage], _gather_pass_on(
            [shard[n].shape[0] for n in stages[stage]], [travelling["zones"][i] for i in which],
            PASS_ON_IDS[stage], "weight_gather_pass_on_%d" % stage)))

    chan = _chan_from_bits(p.pop("chan"))
    r0 = 0
    for n, rows in CHAN:
        p[n] = chan[r0:r0 + rows]
        r0 += rows
    p.update(ln1_g=w["ln1_g"], b_in=w["b_in"] + token[0, 0], rpb=w["rpb"][0], conv_b=w["conv_b"],
             w_rg_a=w["w_rg_a"][0], w_rg_i=w["w_rg_i"][0], ln2_g=w["ln2_g"],
             lnf_g=w["lnf_g"].reshape(1, D))

    core = jnp.reshape(ci, (1,)).astype(jnp.int32)
    chip = jnp.reshape(2 * xi + yi, (1,)).astype(jnp.int32)
    early_sections, late_sections = SECTIONS[1:], SECTIONS[:1]
    in_flight = {}

    def pair_sum_and_send(group, sections, after):
        send_sems, recv_sems, sect, zones, _ = in_flight["pair_" + group]
        sect, got = _pair_exchange_wait(sections, send_sems, recv_sems, sect, zones, after,
                                        "grad_pair_exchange_wait_" + group)
        parts = _pair_add(sections, sect, got, core, "grad_pair_add_" + group)
        in_flight[group] = _chip_exchange_start(sections, parts, "grad_chip_exchange_start_" + group)
        return in_flight[group][-1]

    def reduce_early(grads):
        chan_g = jnp.concatenate([grads[n] for n, _ in CHAN], axis=0)
        chan_g = chan_g.reshape(N_CHAN_ROWS, N_DEV, LANES).transpose(1, 0, 2).astype(BF16)
        chan_g = jnp.pad(chan_g.reshape(N_DEV, -1), ((0, 0), (0, CHAN_BLOCK_ROWS * D - N_CHAN_ROWS * LANES)))
        grads["chan"] = chan_g.reshape(N_DEV * CHAN_BLOCK_ROWS, D)
        grads["gates"] = jnp.concatenate([grads[n].reshape(-1, D) for n in GATE_BLOCKS], axis=0).astype(BF16)
        in_flight["pair_early"] = _pair_exchange_start(
            early_sections, [grads[n] for n, _, _ in early_sections], PAIR_EARLY_ID,
            "grad_pair_exchange_start_early")
        return pair_sum_and_send("early", early_sections, in_flight["pair_early"][-1])[0, 0]

    loss_part, grad_x, grads = _local_step(x[0], loss_target[0], p, late_weights, reduce_early)
    in_flight["pair_late"] = _pair_exchange_start(
        late_sections, [grads[n] for n, _, _ in late_sections], PAIR_LATE_ID, "grad_pair_exchange_start_late")

    def finish(group, sections, after, name):
        send_sems, recv_sems, parts, zones, _ = in_flight[group]
        parts, far = _chip_exchange_wait(sections, send_sems, recv_sems, parts, zones, after,
                                         "grad_chip_exchange_wait_" + name)
        return dict(zip((n for n, _, _ in sections),
                        _grad_finish(sections, parts, far, chip, "grad_finish_" + name)))

    summed = finish("early", early_sections, in_flight["pair_late"][-1], "early")
    started_late = pair_sum_and_send("late", late_sections, summed["gates"])

    flat = jnp.concatenate([grads[n].reshape(-1) for n, _ in REPLICATED if n not in GATE_BLOCKS]
                           + [loss_part.reshape(-1) + started_late[0, 0]])
    n_small = flat.shape[0]
    flat = jnp.pad(flat, (0, SMALL_ROWS * LANES - n_small)).reshape(SMALL_ROWS, LANES)
    small_parts, gate_sum, _ = _all_gather([flat, summed["gates"]], "small_grad_all_gather")
    small = _sum_devices(small_parts, SMALL_ROWS, "small_grad_sum").reshape(-1)
    loss = small[n_small - 1]

    g, delta, new_m, new_v = {}, {}, {}, {}

    def update(n, g2, shape2):
        d2, m2, v2 = _adamw(w[n].reshape(shape2), g2, m[n].reshape(shape2), v[n].reshape(shape2),
                            "adamw_" + n)
        g[n], delta[n], new_m[n], new_v[n] = (a.reshape(w[n].shape) for a in (g2, d2, m2, v2))

    small_params = []
    o = 0
    for n, shape2 in REPLICATED:
        if n in GATE_BLOCKS:
            k, rows = GATE_BLOCKS.index(n), gate_sum.shape[0] // len(GATE_BLOCKS)
            update(n, gate_sum[k * rows:(k + 1) * rows].reshape(shape2), shape2)
        else:
            size = shape2[0] * shape2[1]
            small_params.append((n, small[o:o + size].reshape(shape2), shape2))
            o += size
    chan_back = summed["chan"].reshape(-1)[:N_CHAN_ROWS * LANES].reshape(N_CHAN_ROWS, LANES)
    r0 = 0
    for n, rows in CHAN:
        small_params.append((n, chan_back[r0:r0 + rows], (rows, LANES)))
        r0 += rows
    results = _adamw_small([(w[n].reshape(s2), g2, m[n].reshape(s2), v[n].reshape(s2))
                            for n, g2, s2 in small_params], "adamw_vectors")
    for (n, g2, _), (d2, m2, v2) in zip(small_params, results):
        g[n], delta[n], new_m[n], new_v[n] = (a.reshape(w[n].shape) for a in (g2, d2, m2, v2))

    for n in ROW_SHARDED:
        update(n, summed[n], summed[n].shape)
    for n, t in TRANSPOSED.items():
        if t in summed:
            update(n, summed[t].T, summed[t].shape[::-1])
    summed = finish("late", late_sections, _after_all(list(delta.values()), "updates_done"), "late")
    update("w_in", summed["w_in_t"].T, summed["w_in_t"].shape[::-1])

    return (loss, grad_x[None], *[g[n] for n in NAMES], *[delta[n] for n in NAMES],
            *[new_m[n] for n in NAMES], *[new_v[n] for n in NAMES])
```

```python
import math

import numpy as np
import jax
import jax.numpy as jnp
from jax import lax
from jax.experimental import pallas as pl
from jax.experimental.pallas import tpu as pltpu

F32 = jnp.float32
BF16 = jnp.bfloat16

T = 2048
D = 1024
D_ATT = 512
D_REC = 1024
D_FF = 4096
D_IN = 5632
N_HEADS = 8
DH = 64
GRID_W = 64
ROWS = T // GRID_W
WIN_H = 8
WIN_W = 16
KWIN = WIN_H * GRID_W
N_RPB_R = 2 * WIN_H - 1
N_RPB_C = 2 * WIN_W - 1
N_REC_BLOCKS = 16
REC_BLOCK = 64
CG = 128
N_CG = D_REC // CG
LRU_C = 8.0
EPS = 1e-6
N_DEV = 8
N_CHIPS = 4
LANES = 128

ADAM_LR = 0.001
ADAM_B1 = 0.9
ADAM_B2 = 0.999
ADAM_EPS = 1e-08
ADAM_WD = 0.01
ADAM_STEP = 10

MESH_AXES = ("x", "y", "c")
VMEM_LIMIT = 56 * 1024 * 1024

TILE = 512
DZ_ARRAYS = ((0, 3, 1), (3, 4, 2), (7, 4, 2))
N_DZ_TILES = D_IN // TILE


def _params(**kw):
    return pltpu.CompilerParams(vmem_limit_bytes=VMEM_LIMIT, **kw)


HG = 4
HQ = HG * GRID_W
HC = HG * DH


def _att_tables():
    rq = np.arange(GRID_W)
    kc = np.arange(KWIN) % GRID_W
    win_start = np.clip(rq - WIN_W // 2, 0, GRID_W - WIN_W)
    valid = (kc[None, :] >= win_start[:, None]) & (kc[None, :] < win_start[:, None] + WIN_W)
    same_head = (np.arange(HQ)[:, None] // GRID_W) == (np.arange(HC)[None, :] // DH)
    return valid.astype(np.float32), same_head.astype(np.float32)


def _pair_mask():
    half = np.arange(2 * DH) // DH
    return (half[:, None] == half[None, :]).astype(np.float32)


def _dup_table():
    return np.concatenate([np.eye(REC_BLOCK, dtype=np.float32)] * 2, axis=1)


def _sigmoid(x):
    return 0.5 * jnp.tanh(0.5 * x) + 0.5


def _softplus(x):
    return jnp.maximum(x, 0.0) + jnp.log(1.0 + jnp.exp(-jnp.abs(x)))


def _one_minus_square(log_a, a):
    x = 2.0 * log_a
    series = -x * (1.0 + x * (0.5 + x * (1.0 / 6.0)))
    return jnp.where(x > -0.02, series, 1.0 - a * a)


_GELU_C = math.sqrt(2.0 / math.pi)


def _gelu_and_grad(x):
    x2 = x * x
    inner = _GELU_C * (x + 0.044715 * x * x2)
    t = jnp.tanh(inner)
    g = 0.5 * x * (1.0 + t)
    dg = 0.5 * (1.0 + t) + 0.5 * x * (1.0 - t * t) * _GELU_C * (1.0 + 3.0 * 0.044715 * x2)
    return g, dg


def _dot(a, b):
    return jnp.dot(a, b, preferred_element_type=F32)


def _dot_nt(a, b):
    return lax.dot_general(a, b, (((1,), (1,)), ((), ())), preferred_element_type=F32)


def _dot_tn(a, b):
    return lax.dot_general(a, b, (((0,), (0,)), ((), ())), preferred_element_type=F32)


def _dot_exact(a, b):
    return jnp.dot(a, b, precision=lax.Precision.HIGHEST, preferred_element_type=F32)


def _shift_rows(x, s):
    n = x.shape[0]
    rows = lax.broadcasted_iota(jnp.int32, x.shape, 0)
    y = pltpu.roll(x, s % n, 0)
    if s > 0:
        return jnp.where(rows >= s, y, 0.0)
    return jnp.where(rows < n + s, y, 0.0)


def _rms_bwd(dh, xh, r, g):
    dxh = dh * g
    return r * (dxh - xh * jnp.mean(dxh * xh, axis=-1, keepdims=True))


def _matmul(a, b, mode, out_dtype, name, tm=512, tn=1024, tk=2048):
    if mode == "nn":
        (m, k), (k2, n) = a.shape, b.shape
    elif mode == "nt":
        (m, k), (n, k2) = a.shape, b.shape
    else:
        (k, m), (k2, n) = a.shape, b.shape
    assert k == k2
    tm, tn, tk = min(tm, m), min(tn, n), min(tk, k)
    assert m % tm == 0 and n % tn == 0 and k % tk == 0
    nk = k // tk
    dot = {"nn": _dot, "nt": _dot_nt, "tn": _dot_tn}[mode]

    def body(a_ref, b_ref, o_ref, acc):
        kk = pl.program_id(2)
        part = dot(a_ref[...].astype(BF16), b_ref[...].astype(BF16))
        if nk == 1:
            o_ref[...] = part.astype(out_dtype)
            return

        @pl.when(kk == 0)
        def _():
            acc[...] = part

        @pl.when(kk > 0)
        def _():
            acc[...] += part

        @pl.when(kk == nk - 1)
        def _():
            o_ref[...] = acc[...].astype(out_dtype)

    if mode == "tn":
        a_spec = pl.BlockSpec((tk, tm), lambda i, j, kk: (kk, i))
    else:
        a_spec = pl.BlockSpec((tm, tk), lambda i, j, kk: (i, kk))
    if mode == "nt":
        b_spec = pl.BlockSpec((tn, tk), lambda i, j, kk: (j, kk))
    else:
        b_spec = pl.BlockSpec((tk, tn), lambda i, j, kk: (kk, j))
    return pl.pallas_call(
        body, name=name,
        out_shape=jax.ShapeDtypeStruct((m, n), out_dtype),
        grid=(m // tm, n // tn, nk),
        in_specs=[a_spec, b_spec],
        out_specs=pl.BlockSpec((tm, tn), lambda i, j, kk: (i, j)),
        scratch_shapes=[pltpu.VMEM((tm, tn) if nk > 1 else (8, LANES), F32)],
        compiler_params=_params(dimension_semantics=("parallel", "parallel", "arbitrary")),
    )(a, b)


def _in_proj(x, g1, w_in_t, b_in):
    tm = 512

    def body(x_ref, g_ref, w_hbm, b_ref, qkv_ref, uy_ref, gg_ref, h_ref, w):
        @pl.when(pl.program_id(0) == 0)
        def _():
            pltpu.sync_copy(w_hbm, w)

        xv = x_ref[...]
        r = lax.rsqrt(jnp.mean(xv * xv, axis=-1, keepdims=True) + EPS)
        h = ((xv * r) * g_ref[...]).astype(BF16)
        h_ref[...] = h
        row0 = 0
        for ref in (qkv_ref, uy_ref, gg_ref):
            for c0 in range(0, ref.shape[1], TILE):
                z = _dot_nt(h, w[row0:row0 + TILE, :]) + b_ref[:, row0:row0 + TILE]
                ref[:, c0:c0 + TILE] = z.astype(ref.dtype)
                row0 += TILE

    tok = lambda width: pl.BlockSpec((tm, width), lambda i: (i, 0))
    return pl.pallas_call(
        body, name="in_proj",
        out_shape=(jax.ShapeDtypeStruct((T, 3 * D_ATT), BF16),
                   jax.ShapeDtypeStruct((T, 2 * D_REC), F32),
                   jax.ShapeDtypeStruct((T, 2 * D), F32),
                   jax.ShapeDtypeStruct((T, D), BF16)),
        grid=(T // tm,),
        in_specs=[tok(D), pl.BlockSpec((1, D), lambda i: (0, 0)), pl.BlockSpec(memory_space=pl.ANY),
                  pl.BlockSpec((1, D_IN), lambda i: (0, 0))],
        out_specs=(tok(3 * D_ATT), tok(2 * D_REC), tok(2 * D), tok(D)),
        scratch_shapes=[pltpu.VMEM((D_IN, D), BF16)],
        compiler_params=_params(dimension_semantics=("arbitrary",)),
    )(x, g1, w_in_t, b_in)


def _dz_specs(rows, tile_of, row_of):
    def spec(off, n, per_plane):
        def index(*ids):
            t = jnp.clip(tile_of(*ids) - off, 0, n - 1)
            return (t // per_plane, row_of(*ids), t % per_plane)
        return pl.BlockSpec((1, rows, TILE), index)
    return [spec(off, n, per) for off, n, per in DZ_ARRAYS]


def _dh_norm1_bwd(dz, w_in_t, x, g1, dx1):
    tm = 512

    def body(dqkv_ref, duy_ref, dgg_ref, w_hbm, x_ref, g_ref, dx1_ref, gx_ref, dg_ref, w):
        @pl.when(pl.program_id(0) == 0)
        def _():
            pltpu.sync_copy(w_hbm, w)
            dg_ref[...] = jnp.zeros_like(dg_ref)

        dh, row0 = None, 0
        for ref in (dqkv_ref, duy_ref, dgg_ref):
            for plane in range(ref.shape[0]):
                cols = ref.shape[2]
                part = _dot(ref[plane], w[row0:row0 + cols, :])
                dh = part if dh is None else dh + part
                row0 += cols
        xv = x_ref[...]
        r = lax.rsqrt(jnp.mean(xv * xv, axis=-1, keepdims=True) + EPS)
        xh = xv * r
        dg_ref[...] += jnp.sum(dh * xh, axis=0, keepdims=True)
        gx_ref[...] = dx1_ref[...] + _rms_bwd(dh, xh, r, g_ref[...])

    tok = pl.BlockSpec((tm, D), lambda i: (i, 0))
    vec = pl.BlockSpec((1, D), lambda i: (0, 0))
    planes = lambda a: pl.BlockSpec((a.shape[0], tm, a.shape[2]), lambda i: (0, i, 0))
    return pl.pallas_call(
        body, name="dh_norm1_bwd",
        out_shape=(jax.ShapeDtypeStruct((T, D), F32), jax.ShapeDtypeStruct((1, D), F32)),
        grid=(T // tm,),
        in_specs=[planes(a) for a in dz] + [pl.BlockSpec(memory_space=pl.ANY), tok, vec, tok],
        out_specs=(tok, vec),
        scratch_shapes=[pltpu.VMEM((D_IN, D), BF16)],
        compiler_params=_params(dimension_semantics=("arbitrary",)),
    )(*dz, w_in_t, x, g1, dx1)


def _grad_w_in(dz, h):
    def body(*refs):
        seg_refs = refs[:3]
        h_ref, gw_ref, gb_ref = refs[3:]
        j = pl.program_id(0)

        for s, (off, n, _) in enumerate(DZ_ARRAYS):
            @pl.when((j >= off) & (j < off + n))
            def _(s=s):
                a = seg_refs[s][0]
                gw_ref[...] = _dot_tn(a, h_ref[...]).astype(BF16)
                gb_ref[...] = jnp.sum(a.astype(F32), axis=0, keepdims=True)

    return pl.pallas_call(
        body, name="grad_w_in",
        out_shape=(jax.ShapeDtypeStruct((D_IN, D), BF16), jax.ShapeDtypeStruct((1, D_IN), F32)),
        grid=(N_DZ_TILES,),
        in_specs=_dz_specs(T, lambda j: j, lambda j: 0) + [pl.BlockSpec((T, D), lambda j: (0, 0))],
        out_specs=(pl.BlockSpec((TILE, D), lambda j: (j, 0)), pl.BlockSpec((1, TILE), lambda j: (0, j))),
        compiler_params=_params(dimension_semantics=("parallel",)),
    )(*dz, h)


def _rpb_rows(rpb):
    padded = jnp.pad(rpb, ((0, 0), (0, 0), (0, GRID_W - N_RPB_C)))
    rows = [padded[:, WIN_H - 1 - oi: 2 * WIN_H - 1 - oi].reshape(N_HEADS // HG, HG, KWIN)
            for oi in range(WIN_H)]
    return jnp.stack(rows, axis=0)


SKEW = KWIN - (WIN_W - 1)


MASKED = -1e30


def _bias_tiles(rows_ref, valid, bias_s):
    for oi in range(WIN_H):
        for hh in range(HG):
            row = jnp.broadcast_to(rows_ref[oi, 0, hh:hh + 1, :], (GRID_W, KWIN))
            tile = pltpu.roll(row, SKEW, 1, stride=1, stride_axis=0)
            bias_s[oi, hh * GRID_W:(hh + 1) * GRID_W, :] = jnp.where(valid, tile, MASKED)


def _bias_tile_grads(gb_s, flip, out_ref):
    for oi in range(WIN_H):
        for hh in range(HG):
            g = _dot_exact(flip, gb_s[oi, hh * GRID_W:(hh + 1) * GRID_W, :])
            back = pltpu.roll(g, KWIN - (GRID_W - WIN_W), 1, stride=1, stride_axis=0)
            out_ref[0, oi, hh:hh + 1, :] = jnp.sum(back, axis=0, keepdims=True)


def _rpb_fold(row_grads):
    g = row_grads.transpose(1, 0, 2, 3).reshape(WIN_H, N_HEADS, WIN_H, GRID_W)
    g = g.transpose(0, 2, 1, 3)

    def body(g_ref, o_ref):
        for dr in range(N_RPB_R):
            terms = [g_ref[oi, i] for oi in range(WIN_H) for i in range(WIN_H) if i - oi + WIN_H - 1 == dr]
            acc = terms[0]
            for term in terms[1:]:
                acc = acc + term
            o_ref[dr] = acc

    out = pl.pallas_call(
        body, name="rpb_fold",
        out_shape=jax.ShapeDtypeStruct((N_RPB_R, N_HEADS, GRID_W), F32),
    )(g)
    return out.transpose(1, 0, 2)[:, :, :N_RPB_C]


ATT_GROUPS = N_HEADS // HG
ATT_UNROLL = 8


def _stacked(rows64, same_head):
    return jnp.where(same_head, jnp.concatenate([rows64] * HG, axis=0), jnp.zeros((), BF16))


def _own_heads(stacked):
    head = lax.broadcasted_iota(jnp.int32, (GRID_W, HC), 1) // DH
    out = stacked[:GRID_W]
    for h in range(1, HG):
        out = jnp.where(head == h, stacked[h * GRID_W:(h + 1) * GRID_W], out)
    return out


def _att_scores(q_ref, k_ref, bias_ref, same_head, r):
    rs = jnp.clip(r - WIN_H // 2, 0, ROWS - WIN_H)
    oi = r - rs
    q0 = pl.multiple_of(r * GRID_W, GRID_W)
    k0 = pl.multiple_of(rs * GRID_W, GRID_W)
    q2 = _stacked(q_ref[pl.ds(q0, GRID_W), :] * (DH ** -0.5), same_head)
    kw = k_ref[pl.ds(k0, KWIN), :]
    s = _dot_nt(q2, kw) + bias_ref[oi]
    e = jnp.exp(s - jnp.max(s, axis=-1, keepdims=True))
    return e, 1.0 / jnp.sum(e, axis=-1, keepdims=True), q2, kw, q0, k0, oi


def _att_specs():
    col = lambda off: pl.BlockSpec((T, HC), lambda g: (0, g + off * ATT_GROUPS))
    tables = [pl.BlockSpec((WIN_H, 1, HG, KWIN), lambda g: (0, g, 0, 0)),
              pl.BlockSpec((GRID_W, KWIN), lambda g: (0, 0)),
              pl.BlockSpec((HQ, HC), lambda g: (0, 0))]
    return col, tables, pltpu.VMEM((WIN_H, HQ, KWIN), F32)


def _att_fwd(qkv, bias_rows):
    valid_np, same_head_np = _att_tables()

    def body(q_ref, k_ref, v_ref, rows_ref, valid_ref, head_ref, o_ref, bias_s):
        same_head = head_ref[...] > 0.5
        _bias_tiles(rows_ref, valid_ref[...] > 0.5, bias_s)

        def row(r, carry):
            e, rl, _, _, q0, k0, _ = _att_scores(q_ref, k_ref, bias_s, same_head, r)
            o2 = _dot((e * rl).astype(BF16), v_ref[pl.ds(k0, KWIN), :])
            o_ref[pl.ds(q0, GRID_W), :] = _own_heads(o2).astype(BF16)
            return carry

        lax.fori_loop(0, ROWS, row, 0, unroll=ATT_UNROLL)

    col, tables, tiles = _att_specs()
    return pl.pallas_call(
        body, name="att_fwd",
        out_shape=jax.ShapeDtypeStruct((T, D_ATT), BF16),
        grid=(ATT_GROUPS,),
        in_specs=[col(0), col(1), col(2)] + tables,
        out_specs=col(0),
        scratch_shapes=[tiles],
        compiler_params=_params(dimension_semantics=("parallel",)),
    )(qkv, qkv, qkv, bias_rows, jnp.asarray(valid_np), jnp.asarray(same_head_np))


def _att_bwd(qkv, bias_rows, datt, after):
    valid_np, same_head_np = _att_tables()

    def body(q_ref, k_ref, v_ref, do_ref, rows_ref, valid_ref, head_ref, flip_ref,
             dqkv_ref, grows_ref, dk_acc, dv_acc, bias_s, gb_s):
        same_head = head_ref[...] > 0.5
        dk_acc[...] = jnp.zeros_like(dk_acc)
        dv_acc[...] = jnp.zeros_like(dv_acc)
        gb_s[...] = jnp.zeros_like(gb_s)
        _bias_tiles(rows_ref, valid_ref[...] > 0.5, bias_s)

        def row(r, carry):
            e, rl, q2, kw, q0, k0, oi = _att_scores(q_ref, k_ref, bias_s, same_head, r)
            do2 = _stacked(do_ref[pl.ds(q0, GRID_W), :], same_head)
            vw = v_ref[pl.ds(k0, KWIN), :]
            p = e * rl
            dp = _dot_nt(do2, vw)
            ds = p * (dp - jnp.sum(dp * p, axis=-1, keepdims=True))
            p16 = p.astype(BF16)
            ds16 = ds.astype(BF16)
            dv_acc[pl.ds(k0, KWIN), :] += _dot_tn(p16, do2)
            dk_acc[pl.ds(k0, KWIN), :] += _dot_tn(ds16, q2)
            dq2 = _dot(ds16, kw) * (DH ** -0.5)
            dqkv_ref[0, pl.ds(q0, GRID_W), :] = _own_heads(dq2).astype(BF16)
            gb_s[oi] += ds
            return carry

        lax.fori_loop(0, ROWS, row, 0, unroll=ATT_UNROLL)
        dqkv_ref[1] = dk_acc[...].astype(BF16)
        dqkv_ref[2] = dv_acc[...].astype(BF16)
        _bias_tile_grads(gb_s, flip_ref[...], grows_ref)

    col, tables, tiles = _att_specs()
    return pl.pallas_call(
        body, name="att_bwd",
        out_shape=(jax.ShapeDtypeStruct((3, T, D_ATT), BF16),
                   jax.ShapeDtypeStruct((ATT_GROUPS, WIN_H, HG, KWIN), F32)),
        grid=(ATT_GROUPS,),
        in_specs=[col(0), col(1), col(2), col(0)] + tables + [pl.BlockSpec((GRID_W, GRID_W), lambda g: (0, 0))],
        out_specs=(pl.BlockSpec((3, T, HC), lambda g: (0, 0, g)),
                   pl.BlockSpec((1, WIN_H, HG, KWIN), lambda g: (g, 0, 0, 0))),
        scratch_shapes=[pltpu.VMEM((T, HC), F32), pltpu.VMEM((T, HC), F32), tiles, tiles],
        compiler_params=_params(dimension_semantics=("parallel",)),
    )(qkv, qkv, qkv, datt, bias_rows, jnp.asarray(valid_np) + after, jnp.asarray(same_head_np),
      jnp.asarray(np.eye(GRID_W, dtype=np.float32)[::-1].copy()))


def _conv_taps(up):
    return (_shift_rows(up, 2), _shift_rows(up, 1), up, _shift_rows(up, -1))


def _pair_block_diag(w_pair, dup, same_half):
    return jnp.where(same_half, _dot(w_pair.astype(BF16), dup), 0.0).astype(BF16)


def _gates(u, u16, wa, ba, wi, bi, lam):
    r = _sigmoid(_dot(u16, wa) + ba)
    ig = _sigmoid(_dot(u16, wi) + bi)
    sp = _softplus(-lam)
    log_a = (-LRU_C) * r * sp
    a = jnp.exp(log_a)
    mult2 = jnp.maximum(_one_minus_square(log_a, a), 0.0)
    return r, ig, sp, a, jnp.sqrt(mult2), mult2


SCAN_BLOCKS = 8


def _scans(jobs):
    c = jobs[0][0].shape[1]
    nblk = T // 8
    rows = lax.broadcasted_iota(jnp.int32, (8, c), 0)

    def block(a, b, reverse):
        for s in (1, 2, 4):
            if reverse:
                keep = rows < 8 - s
                a_s = jnp.where(keep, pltpu.roll(a, 8 - s, 0), 1.0)
                b_s = jnp.where(keep, pltpu.roll(b, 8 - s, 0), 0.0)
            else:
                keep = rows >= s
                a_s = jnp.where(keep, pltpu.roll(a, s, 0), 1.0)
                b_s = jnp.where(keep, pltpu.roll(b, s, 0), 0.0)
            b = a * b_s + b
            a = a * a_s
        return a, b

    def step(i, carry):
        out = []
        for (a_ref, b_ref, h_ref, reverse), h_prev in zip(jobs, carry):
            for u in range(SCAN_BLOCKS):
                blk = i * SCAN_BLOCKS + u
                if reverse:
                    blk = nblk - 1 - blk
                t0 = pl.multiple_of(blk * 8, 8)
                a, b = block(a_ref[pl.ds(t0, 8), :], b_ref[pl.ds(t0, 8), :], reverse)
                h = a * h_prev + b
                h_ref[pl.ds(t0, 8), :] = h
                h_prev = jnp.broadcast_to(h[0:1] if reverse else h[7:8], (8, c))
            out.append(h_prev)
        return tuple(out)

    lax.fori_loop(0, nblk // SCAN_BLOCKS, step, tuple(jnp.zeros((8, c), F32) for _ in jobs))


def _rec_specs():
    tok = lambda off: pl.BlockSpec((T, CG), lambda g: (0, g + off))
    per_ch = lambda rows: pl.BlockSpec((rows, CG), lambda g: (0, g))
    wspec = pl.BlockSpec((2, 1, CG, REC_BLOCK), lambda g: (0, g, 0, 0))
    const = lambda shape: pl.BlockSpec(shape, lambda g: (0, 0))
    return tok, per_ch, wspec, const


def _rec_fwd(uy, conv_w, conv_b, w_a, b_a, w_i, b_i, lam):
    tok, per_ch, wspec, const = _rec_specs()

    def body(up_ref, yb_ref, cw_ref, cb_ref, wa_ref, ba_ref, wi_ref, bi_ref, lam_ref, dup_ref, half_ref,
             hf_ref, hb_ref, yrec_ref, am_ref, bx_f, bx_b):
        dup = dup_ref[...]
        same_half = half_ref[...] > 0.5
        taps = _conv_taps(up_ref[...])
        u = cb_ref[...]
        for j in range(4):
            u = u + taps[j] * cw_ref[j:j + 1, :]
        u16 = u.astype(BF16)
        for d, bx_s in enumerate((bx_f, bx_b)):
            wa = _pair_block_diag(wa_ref[d, 0], dup, same_half)
            wi = _pair_block_diag(wi_ref[d, 0], dup, same_half)
            _, ig, _, a, mult, _ = _gates(u, u16, wa, ba_ref[d:d + 1, :], wi, bi_ref[d:d + 1, :],
                                       lam_ref[d:d + 1, :])
            am_ref[2 * d] = a
            am_ref[2 * d + 1] = mult
            bx_s[...] = mult * (ig * u)
        _scans([(am_ref.at[0], bx_f, hf_ref, False), (am_ref.at[2], bx_b, hb_ref, True)])
        gelu, _ = _gelu_and_grad(yb_ref[...])
        yrec_ref[...] = ((hf_ref[...] + hb_ref[...]) * gelu).astype(BF16)

    return pl.pallas_call(
        body, name="rec_fwd",
        out_shape=(jax.ShapeDtypeStruct((T, D_REC), F32), jax.ShapeDtypeStruct((T, D_REC), F32),
                   jax.ShapeDtypeStruct((T, D_REC), BF16), jax.ShapeDtypeStruct((4, T, D_REC), F32)),
        grid=(N_CG,),
        in_specs=[tok(0), tok(N_CG), per_ch(4), per_ch(1), wspec, per_ch(2), wspec, per_ch(2), per_ch(2),
                  const((REC_BLOCK, CG)), const((CG, CG))],
        out_specs=(tok(0), tok(0), tok(0), pl.BlockSpec((4, T, CG), lambda g: (0, 0, g))),
        scratch_shapes=[pltpu.VMEM((T, CG), F32)] * 2,
        compiler_params=_params(dimension_semantics=("parallel",)),
    )(uy, uy, conv_w, conv_b, w_a, b_a, w_i, b_i, lam,
      jnp.asarray(_dup_table(), BF16), jnp.asarray(_pair_mask()))


def _rec_bwd(uy, hf, hb, am, dyrec, conv_w, conv_b, w_a, b_a, w_i, b_i, lam):
    tok, per_ch, wspec, const = _rec_specs()

    def body(up_ref, yb_ref, hf_ref, hb_ref, am_ref, dy_ref, cw_ref, cb_ref, wa_ref, ba_ref, wi_ref, bi_ref,
             lam_ref, dup_ref, dupt_ref, half_ref,
             duy_ref, dcw_ref, dcb_ref, dwa_ref, dba_ref, dwi_ref, dbi_ref, dlam_ref,
             a_s0, a_s1, dh_s, g_s0, g_s1):
        dup = dup_ref[...]
        dup_t = dupt_ref[...]
        same_half = half_ref[...] > 0.5
        taps = _conv_taps(up_ref[...])
        u = cb_ref[...]
        for j in range(4):
            u = u + taps[j] * cw_ref[j:j + 1, :]
        u16 = u.astype(BF16)
        gelu, dgelu = _gelu_and_grad(yb_ref[...])
        dy = dy_ref[...]
        duy_ref[1] = (dy * (hf_ref[...] + hb_ref[...]) * dgelu).astype(BF16)
        dh_s[...] = dy * gelu
        a_s0[...] = _shift_rows(am_ref[0], -1)
        a_s1[...] = _shift_rows(am_ref[2], 1)
        _scans([(a_s0, dh_s, g_s0, True), (a_s1, dh_s, g_s1, False)])
        du = jnp.zeros((T, CG), F32)
        for d, g_s in enumerate((g_s0, g_s1)):
            reverse = d == 1
            wa = _pair_block_diag(wa_ref[d, 0], dup, same_half)
            wi = _pair_block_diag(wi_ref[d, 0], dup, same_half)
            lam_d = lam_ref[d:d + 1, :]
            r = _sigmoid(_dot(u16, wa) + ba_ref[d:d + 1, :])
            ig = _sigmoid(_dot(u16, wi) + bi_ref[d:d + 1, :])
            sp = _softplus(-lam_d)
            a, mult = am_ref[2 * d], am_ref[2 * d + 1]
            mult2 = mult * mult
            g = g_s[...]
            h_prev = _shift_rows(hb_ref[...], -1) if reverse else _shift_rows(hf_ref[...], 1)
            da = g * h_prev
            dmult = g * (ig * u)
            dig = g * mult * u
            du = du + g * mult * ig
            dmult_dlog = jnp.where(mult2 > 0.0, -(a * a) * lax.rsqrt(mult2), 0.0)
            dlog_a = da * a + dmult * dmult_dlog
            dr = dlog_a * ((-LRU_C) * sp)
            dsp = jnp.sum(dlog_a * ((-LRU_C) * r), axis=0, keepdims=True)
            dlam_ref[d:d + 1, :] = dsp * (-_sigmoid(-lam_d))
            dga = dr * r * (1.0 - r)
            dgi = dig * ig * (1.0 - ig)
            dga16 = dga.astype(BF16)
            dgi16 = dgi.astype(BF16)
            du = du + _dot_nt(dga16, wa) + _dot_nt(dgi16, wi)
            dwa_ref[d, 0] = _dot_exact(jnp.where(same_half, _dot_tn(u16, dga16), 0.0), dup_t)
            dwi_ref[d, 0] = _dot_exact(jnp.where(same_half, _dot_tn(u16, dgi16), 0.0), dup_t)
            dba_ref[d:d + 1, :] = jnp.sum(dga, axis=0, keepdims=True)
            dbi_ref[d:d + 1, :] = jnp.sum(dgi, axis=0, keepdims=True)
        dcb_ref[...] = jnp.sum(du, axis=0, keepdims=True)
        for j in range(4):
            dcw_ref[j:j + 1, :] = jnp.sum(du * taps[j], axis=0, keepdims=True)
        dup_in = (_shift_rows(du, -2) * cw_ref[0:1, :] + _shift_rows(du, -1) * cw_ref[1:2, :]
                  + du * cw_ref[2:3, :] + _shift_rows(du, 1) * cw_ref[3:4, :])
        duy_ref[0] = dup_in.astype(BF16)

    wshape = jax.ShapeDtypeStruct((2, N_CG, CG, REC_BLOCK), F32)
    vec = lambda rows: jax.ShapeDtypeStruct((rows, D_REC), F32)
    dup_np = _dup_table()
    return pl.pallas_call(
        body, name="rec_bwd",
        out_shape=(jax.ShapeDtypeStruct((2, T, D_REC), BF16),
                   vec(4), vec(1), wshape, vec(2), wshape, vec(2), vec(2)),
        grid=(N_CG,),
        in_specs=[tok(0), tok(N_CG), tok(0), tok(0), pl.BlockSpec((4, T, CG), lambda g: (0, 0, g)), tok(0),
                  per_ch(4), per_ch(1), wspec, per_ch(2), wspec, per_ch(2), per_ch(2),
                  const((REC_BLOCK, CG)), const((CG, REC_BLOCK)), const((CG, CG))],
        out_specs=(pl.BlockSpec((2, T, CG), lambda g: (0, 0, g)),
                   per_ch(4), per_ch(1), wspec, per_ch(2), wspec, per_ch(2), per_ch(2)),
        scratch_shapes=[pltpu.VMEM((T, CG), F32)] * 5,
        compiler_params=_params(dimension_semantics=("parallel",)),
    )(uy, uy, hf, hb, am, dyrec, conv_w, conv_b, w_a, b_a, w_i, b_i, lam,
      jnp.asarray(dup_np, BF16), jnp.asarray(dup_np.T.copy()), jnp.asarray(_pair_mask()))


TM_MIX = 256


def _mix_specs():
    tok = lambda width, blk=0: pl.BlockSpec((TM_MIX, width), lambda i: (i, blk))
    full = lambda shape: pl.BlockSpec(shape, lambda i: (0, 0))
    return tok, full


def _mix_fwd(x, att, yrec, gg, w_att_o_t, w_rec_o, w_out):
    tok, full = _mix_specs()

    def body(x_ref, att_ref, yr_ref, ga_ref, gr_ref, wao_ref, wro_ref, wo_ref, x1_ref, mixed_ref):
        y_att = _dot_nt(att_ref[...], wao_ref[...])
        y_rec = _dot(yr_ref[...], wro_ref[...])
        mixed = (_sigmoid(ga_ref[...]) * y_att + _sigmoid(gr_ref[...]) * y_rec).astype(BF16)
        mixed_ref[...] = mixed
        x1_ref[...] = x_ref[...] + _dot(mixed, wo_ref[...])

    return pl.pallas_call(
        body, name="mix_fwd",
        out_shape=(jax.ShapeDtypeStruct((T, D), F32), jax.ShapeDtypeStruct((T, D), BF16)),
        grid=(T // TM_MIX,),
        in_specs=[tok(D), tok(D_ATT), tok(D_REC), tok(D, 0), tok(D, 1),
                  full((D, D_ATT)), full((D_REC, D)), full((D, D))],
        out_specs=(tok(D), tok(D)),
        compiler_params=_params(dimension_semantics=("parallel",)),
    )(x, att, yrec, gg, gg, w_att_o_t, w_rec_o, w_out)


def _mix_bwd(dx1, att, yrec, gg, w_att_o_t, w_rec_o, w_out):
    tok, full = _mix_specs()

    def body(dx_ref, att_ref, yr_ref, ga_ref, gr_ref, wao_ref, wro_ref, wo_ref,
             dgg_ref, dya_ref, dyr_ref, datt_ref, dyrp_ref):
        dmixed = _dot_nt(dx_ref[...].astype(BF16), wo_ref[...])
        y_att = _dot_nt(att_ref[...], wao_ref[...])
        y_rec = _dot(yr_ref[...], wro_ref[...])
        sa = _sigmoid(ga_ref[...])
        sr = _sigmoid(gr_ref[...])
        dgg_ref[0] = (dmixed * y_att * sa * (1.0 - sa)).astype(BF16)
        dgg_ref[1] = (dmixed * y_rec * sr * (1.0 - sr)).astype(BF16)
        dya = (dmixed * sa).astype(BF16)
        dyr = (dmixed * sr).astype(BF16)
        dya_ref[...] = dya
        dyr_ref[...] = dyr
        datt_ref[...] = _dot(dya, wao_ref[...]).astype(BF16)
        dyrp_ref[...] = _dot_nt(dyr, wro_ref[...])

    return pl.pallas_call(
        body, name="mix_bwd",
        out_shape=(jax.ShapeDtypeStruct((2, T, D), BF16),
                   jax.ShapeDtypeStruct((T, D), BF16), jax.ShapeDtypeStruct((T, D), BF16),
                   jax.ShapeDtypeStruct((T, D_ATT), BF16), jax.ShapeDtypeStruct((T, D_REC), F32)),
        grid=(T // TM_MIX,),
        in_specs=[tok(D), tok(D_ATT), tok(D_REC), tok(D, 0), tok(D, 1),
                  full((D, D_ATT)), full((D_REC, D)), full((D, D))],
        out_specs=(pl.BlockSpec((2, TM_MIX, D), lambda i: (0, i, 0)),
                   tok(D), tok(D), tok(D_ATT), tok(D_REC)),
        compiler_params=_params(dimension_semantics=("parallel",)),
    )(dx1, att, yrec, gg, gg, w_att_o_t, w_rec_o, w_out)


TM_FFN = 256
FF_CHUNK = 1024


def _ffn_loss(x1, target, g2, gf, w_ff1_t, w_ff2):
    n_chunks = D_FF // FF_CHUNK

    def body(x1_ref, tg_ref, g2_ref, gf_ref, w1_hbm, w2_hbm,
             loss_ref, dx1_ref, h2_ref, act_ref, dpre_ref, dx2_ref, dg2_ref, dgf_ref,
             w1, w2, relu_s):
        i = pl.program_id(0)

        @pl.when(i == 0)
        def _():
            pltpu.sync_copy(w1_hbm, w1)
            pltpu.sync_copy(w2_hbm, w2)
            loss_ref[...] = jnp.zeros_like(loss_ref)
            dg2_ref[...] = jnp.zeros_like(dg2_ref)
            dgf_ref[...] = jnp.zeros_like(dgf_ref)

        x1v = x1_ref[...]
        r2 = lax.rsqrt(jnp.mean(x1v * x1v, axis=-1, keepdims=True) + EPS)
        xh2 = x1v * r2
        h2 = (xh2 * g2_ref[...]).astype(BF16)
        h2_ref[...] = h2
        x2 = x1v
        for c in range(n_chunks):
            ff = slice(c * FF_CHUNK, (c + 1) * FF_CHUNK)
            rl = jnp.maximum(_dot_nt(h2, w1[ff, :]), 0.0)
            relu_s[:, ff] = rl
            act = (rl * rl).astype(BF16)
            act_ref[:, ff] = act
            x2 = x2 + _dot(act, w2[ff, :])
        r3 = lax.rsqrt(jnp.mean(x2 * x2, axis=-1, keepdims=True) + EPS)
        xh3 = x2 * r3
        err = xh3 * gf_ref[...] - tg_ref[...]
        loss_ref[...] += 0.5 * jnp.sum(jnp.mean(err * err, axis=-1, keepdims=True))
        dy = err * (1.0 / D)
        dgf_ref[...] += jnp.sum(dy * xh3, axis=0, keepdims=True)
        dx2 = _rms_bwd(dy, xh3, r3, gf_ref[...])
        dx2_16 = dx2.astype(BF16)
        dx2_ref[...] = dx2_16
        dh2 = jnp.zeros((TM_FFN, D), F32)
        for c in range(n_chunks):
            ff = slice(c * FF_CHUNK, (c + 1) * FF_CHUNK)
            dpre = (_dot_nt(dx2_16, w2[ff, :]) * (2.0 * relu_s[:, ff])).astype(BF16)
            dpre_ref[:, ff] = dpre
            dh2 = dh2 + _dot(dpre, w1[ff, :])
        dg2_ref[...] += jnp.sum(dh2 * xh2, axis=0, keepdims=True)
        dx1_ref[...] = dx2 + _rms_bwd(dh2, xh2, r2, g2_ref[...])

    tok = lambda width: pl.BlockSpec((TM_FFN, width), lambda i: (i, 0))
    vec = pl.BlockSpec((1, D), lambda i: (0, 0))
    hbm = pl.BlockSpec(memory_space=pl.ANY)
    return pl.pallas_call(
        body, name="ffn_loss",
        out_shape=(jax.ShapeDtypeStruct((8, 128), F32), jax.ShapeDtypeStruct((T, D), F32),
                   jax.ShapeDtypeStruct((T, D), BF16), jax.ShapeDtypeStruct((T, D_FF), BF16),
                   jax.ShapeDtypeStruct((T, D_FF), BF16), jax.ShapeDtypeStruct((T, D), BF16),
                   jax.ShapeDtypeStruct((1, D), F32), jax.ShapeDtypeStruct((1, D), F32)),
        grid=(T // TM_FFN,),
        in_specs=[tok(D), tok(D), vec, vec, hbm, hbm],
        out_specs=(pl.BlockSpec((8, 128), lambda i: (0, 0)), tok(D), tok(D), tok(D_FF), tok(D_FF), tok(D),
                   vec, vec),
        scratch_shapes=[pltpu.VMEM((D_FF, D), BF16), pltpu.VMEM((D_FF, D), BF16),
                        pltpu.VMEM((TM_FFN, D_FF), F32)],
        compiler_params=_params(dimension_semantics=("arbitrary",)),
    )(x1, target, g2, gf, w_ff1_t, w_ff2)


def _local_step(x, target, p, late_weights, reduce_early):
    bias = _rpb_rows(p["rpb"])
    pairs = lambda w: w.reshape(2, N_CG, CG, REC_BLOCK)
    w_a, w_i = pairs(p["w_rg_a"]), pairs(p["w_rg_i"])
    rec_params = (p["conv_w"], p["conv_b"], w_a, p["b_rg_a"], w_i, p["b_rg_i"], p["lru_lambda"])

    qkv, uy, gg, h = _in_proj(x, p["ln1_g"], p["w_in_t"], p["b_in"])
    att = _att_fwd(qkv, bias)
    hf, hb, yrec, am = _rec_fwd(uy, *rec_params)
    p = {**p, **late_weights(yrec, 0)}
    x1, mixed = _mix_fwd(x, att, yrec, gg, p["w_att_o_t"], p["w_rec_o"], p["w_out"])
    p = {**p, **late_weights(x1, 1)}
    loss8, dx1, h2, act, dpre, dx2, g_ln2, g_lnf = _ffn_loss(
        x1, target, p["ln2_g"], p["lnf_g"], p["w_ff1_t"], p["w_ff2"])

    dgg, dya, dyr, datt, dyrp = _mix_bwd(dx1, att, yrec, gg, p["w_att_o_t"], p["w_rec_o"], p["w_out"])
    duy, g_cw, g_cb, g_wa, g_ba, g_wi, g_bi, g_lam = _rec_bwd(uy, hf, hb, am, dyrp, *rec_params)
    blocks = lambda g: g.reshape(2, N_REC_BLOCKS, REC_BLOCK, REC_BLOCK)
    grads = {
        "w_att_o_t": _matmul(dya, att, "tn", BF16, "g_w_att_o"),
        "conv_w": g_cw, "conv_b": g_cb, "w_rg_a": blocks(g_wa), "b_rg_a": g_ba,
        "w_rg_i": blocks(g_wi), "b_rg_i": g_bi, "lru_lambda": g_lam,
        "w_rec_o": _matmul(yrec, dyr, "tn", BF16, "g_w_rec_o"),
        "w_out": _matmul(mixed, dx1, "tn", BF16, "g_w_out"),
        "ln2_g": g_ln2,
        "w_ff1_t": _matmul(dpre, h2, "tn", BF16, "g_w_ff1"),
        "w_ff2": _matmul(act, dx2, "tn", BF16, "g_w_ff2"),
        "lnf_g": g_lnf,
    }
    dqkv, gbias = _att_bwd(qkv, bias, datt, reduce_early(grads))
    dz = (dqkv, duy, dgg)
    grad_x, g_ln1 = _dh_norm1_bwd(dz, p["w_in_t"], x, p["ln1_g"], dx1)
    g_w_in_t, g_b_in = _grad_w_in(dz, h)
    grads.update(ln1_g=g_ln1, w_in_t=g_w_in_t, b_in=g_b_in, rpb=_rpb_fold(gbias))
    return loss8[0:1, 0:1], grad_x, grads


MESH_ID = pl.DeviceIdType.MESH
ANY = pl.BlockSpec(memory_space=pl.ANY)

CHAN_BLOCK_ROWS = 32
GATE_ROWS = 2 * 2 * N_REC_BLOCKS * REC_BLOCK * REC_BLOCK // (N_DEV * D)
SECTIONS = (("w_in_t", 704, D), ("w_rec_o", 128, D), ("w_out", 128, D), ("w_ff1_t", 512, D),
            ("w_ff2", 512, D), ("chan", CHAN_BLOCK_ROWS, D), ("w_att_o_t", 128, D_ATT),
            ("gates", GATE_ROWS, D))
N_SEC = len(SECTIONS)
N_CHAN_ROWS = 10
CHAN = (("conv_w", 4), ("b_rg_a", 2), ("b_rg_i", 2), ("lru_lambda", 2))


def _position():
    return lax.axis_index("x"), lax.axis_index("y"), lax.axis_index("c")


def _other_chips(x, y):
    return [(1 - x, y), (x, 1 - y), (1 - x, 1 - y)]


PASS_ON_IDS, PAIR_EARLY_ID, PAIR_LATE_ID = (1, 4), 2, 3


def _pair_handshake(x, y, c):
    barrier = pltpu.get_barrier_semaphore()
    pl.semaphore_signal(barrier, inc=1, device_id=(x, y, 1 - c), device_id_type=MESH_ID)
    pl.semaphore_wait(barrier, 1)


def _block_of(ref, dev, rows):
    return ref.at[pl.ds(pl.multiple_of(dev * rows, 16), rows)]


def _all_gather(shards, name):
    ns = len(shards)

    def body(*refs):
        x_refs, out_refs, done_ref = refs[:ns], refs[ns:2 * ns], refs[2 * ns]
        send_sems, recv_sems, local_sems = refs[2 * ns + 1:]
        done_ref[0, 0] = 0.0
        x, y, c = _position()
        me, sibling = (x, y, c), (x, y, 1 - c)
        x_nbr, y_nbr, diagonal = _other_chips(x, y)
        north = c == 1
        relay_from = (jnp.where(north, x_nbr[0], y_nbr[0]), jnp.where(north, x_nbr[1], y_nbr[1]))
        relay_to = (jnp.where(north, y_nbr[0], x_nbr[0]), jnp.where(north, y_nbr[1], x_nbr[1]))

        def rows(s, px, py, pc):
            return _block_of(out_refs[s], 4 * px + 2 * py + pc, shards[s].shape[0])

        def copy(k, s, block, to, from_shard=False):
            return pltpu.make_async_remote_copy(
                src_ref=x_refs[s] if from_shard else rows(s, *block), dst_ref=rows(s, *block),
                send_sem=send_sems.at[k * ns + s], recv_sem=recv_sems.at[k * ns + s],
                device_id=to, device_id_type=MESH_ID)

        sections = range(ns)
        mine = [pltpu.make_async_copy(x_refs[s], rows(s, *me), local_sems.at[s]) for s in sections]
        sent = [copy(k, s, me, to, True) for k, to in enumerate((sibling, (*x_nbr, c), (*y_nbr, c)))
                for s in sections]
        for cp in mine + sent:
            cp.start()
        for s in sections:
            copy(1, s, (*x_nbr, c), me).wait_recv()
            copy(2, s, (*y_nbr, c), me).wait_recv()
            sent += [copy(3, s, (*relay_from, c), (*relay_to, c)),
                     copy(4, s, (*x_nbr, c), sibling), copy(5, s, (*y_nbr, c), sibling)]
            for cp in sent[-3:]:
                cp.start()
        for s in sections:
            copy(3, s, (*diagonal, c), me).wait_recv()
            sent.append(copy(6, s, (*diagonal, c), sibling))
            sent[-1].start()
        for s in sections:
            copy(0, s, sibling, me).wait_recv()
            for k, chip in ((4, x_nbr), (5, y_nbr), (6, diagonal)):
                copy(k, s, (*chip, 1 - c), me).wait_recv()
        for cp in sent:
            cp.wait_send()
        for cp in mine:
            cp.wait()

    return pl.pallas_call(
        body, name=name,
        out_shape=tuple(jax.ShapeDtypeStruct((N_DEV * s.shape[0], s.shape[1]), s.dtype) for s in shards)
        + (jax.ShapeDtypeStruct((1, 1), F32),),
        in_specs=[ANY] * ns,
        out_specs=(ANY,) * ns + (pl.BlockSpec(memory_space=pltpu.SMEM),),
        scratch_shapes=[pltpu.SemaphoreType.DMA((7 * ns,)), pltpu.SemaphoreType.DMA((7 * ns,)),
                        pltpu.SemaphoreType.DMA((ns,))],
    )(*shards)


HBM = pl.BlockSpec(memory_space=pltpu.HBM)
SEM = pl.BlockSpec(memory_space=pltpu.SEMAPHORE)
EFFECT = pltpu.SideEffectType.DATAFLOW_SIDE_EFFECTING


def _in_hbm(a):
    return pltpu.with_memory_space_constraint(a, pltpu.HBM)


def _first_hop_copies(shards, x_refs, zones, send_sems, recv_sems):
    ns = len(shards)
    x, y, c = _position()
    targets = [(x, y, 1 - c)] + [(cx, cy, c) for cx, cy in _other_chips(x, y)]
    return [pltpu.make_async_remote_copy(
        src_ref=x_refs[s], dst_ref=_block_of(zones[s], 4 * x + 2 * y + c, shards[s].shape[0]),
        send_sem=send_sems.at[k * ns + s], recv_sem=recv_sems.at[k * ns + s],
        device_id=to, device_id_type=MESH_ID)
        for k, to in enumerate(targets) for s in range(ns)]


def _after_all(arrays, name):
    def body(*refs):
        refs[-1][...] = jnp.zeros_like(refs[-1])

    return pl.pallas_call(
        body, name=name,
        out_shape=jax.ShapeDtypeStruct((8, LANES), F32),
        in_specs=[pl.BlockSpec(memory_space=pl.ANY)] * len(arrays),
        out_specs=pl.BlockSpec(memory_space=pltpu.VMEM),
    )(*arrays)


def _own_blocks_placed(shards, after):
    ns = len(shards)
    x, y, c = _position()
    me = jnp.reshape(4 * x + 2 * y + c, (1,)).astype(jnp.int32)
    shards = [*shards[:-1], shards[-1] + after.astype(shards[-1].dtype)]

    def body(me_ref, *refs):
        for s in range(ns):
            refs[ns + s][...] = refs[s][...]

    return pl.pallas_call(
        body, name="own_blocks_placed",
        out_shape=tuple(jax.ShapeDtypeStruct((N_DEV * s.shape[0], s.shape[1]), s.dtype) for s in shards),
        grid_spec=pltpu.PrefetchScalarGridSpec(
            num_scalar_prefetch=1, grid=(1,),
            in_specs=[pl.BlockSpec(s.shape, lambda i, me: (0, 0)) for s in shards],
            out_specs=tuple(pl.BlockSpec(s.shape, lambda i, me: (me[0], 0)) for s in shards)),
        compiler_params=_params(dimension_semantics=("arbitrary",)),
    )(me, *shards)


def _gather_start(shards, after, name):
    ns = len(shards)
    zones = _own_blocks_placed(shards, after)

    def body(*refs):
        for cp in _first_hop_copies(shards, refs[:ns], refs[ns:2 * ns], refs[2 * ns], refs[2 * ns + 1]):
            cp.start()
        refs[-1][...] = jnp.zeros_like(refs[-1])

    out = pl.pallas_call(
        body, name=name,
        out_shape=(pltpu.SemaphoreType.DMA((4 * ns,)), pltpu.SemaphoreType.DMA((4 * ns,)),
                   *[pltpu.HBM(a.shape, a.dtype) for a in (*shards, *zones)],
                   jax.ShapeDtypeStruct((8, LANES), F32)),
        in_specs=[HBM] * (2 * ns),
        out_specs=(SEM, SEM, *[HBM] * (2 * ns), pl.BlockSpec(memory_space=pltpu.VMEM)),
        input_output_aliases={i: 2 + i for i in range(2 * ns)},
        compiler_params=pltpu.CompilerParams(has_side_effects=EFFECT),
    )(*[_in_hbm(a) for a in shards], *[_in_hbm(a) for a in zones])
    return out[0], out[1], out[2:2 + ns], out[2 + ns:2 + 2 * ns], out[-1]


def _gather_wait(send_sems, recv_sems, shards, zones, which, after, name):
    ns = len(shards)

    def body(*refs):
        copies = _first_hop_copies(shards, refs[:ns], refs[ns:2 * ns], refs[2 * ns], refs[2 * ns + 1])
        for i, cp in enumerate(copies):
            if i % ns in which:
                cp.wait_send()
                cp.wait_recv()

    out = pl.pallas_call(
        body, name=name,
        out_shape=tuple(pltpu.HBM(a.shape, a.dtype) for a in (*shards, *zones)),
        in_specs=[HBM] * (2 * ns) + [SEM, SEM, ANY],
        out_specs=(HBM,) * (2 * ns),
        input_output_aliases={i: i for i in range(2 * ns)},
        compiler_params=pltpu.CompilerParams(has_side_effects=EFFECT),
    )(*shards, *zones, send_sems, recv_sems, after)
    return out[:ns], out[ns:]


def _gather_pass_on(rows, zones, barrier_id, name):
    ns = len(zones)

    def body(*refs):
        in_refs, out_refs = refs[:ns], refs[ns:2 * ns]
        send_sems, recv_sems = refs[2 * ns:]
        x, y, c = _position()
        _pair_handshake(x, y, c)
        copies = [pltpu.make_async_remote_copy(
            src_ref=_block_of(in_refs[s], 4 * cx + 2 * cy + c, rows[s]),
            dst_ref=_block_of(out_refs[s], 4 * cx + 2 * cy + c, rows[s]),
            send_sem=send_sems.at[j * ns + s], recv_sem=recv_sems.at[j * ns + s],
            device_id=(x, y, 1 - c), device_id_type=MESH_ID)
            for j, (cx, cy) in enumerate(_other_chips(x, y)) for s in range(ns)]
        for cp in copies:
            cp.start()
        for cp in copies:
            cp.wait_recv()
        for cp in copies:
            cp.wait_send()

    return pl.pallas_call(
        body, name=name,
        out_shape=tuple(jax.ShapeDtypeStruct(z.shape, z.dtype) for z in zones),
        in_specs=[ANY] * ns, out_specs=(ANY,) * ns,
        input_output_aliases={i: i for i in range(ns)},
        scratch_shapes=[pltpu.SemaphoreType.DMA((3 * ns,)), pltpu.SemaphoreType.DMA((3 * ns,))],
        compiler_params=pltpu.CompilerParams(collective_id=barrier_id),
    )(*zones)


def _pair_copies(sections, g_refs, land, send_sems, recv_sems):
    ns = len(sections)
    x, y, c = _position()
    return [pltpu.make_async_remote_copy(
        src_ref=_block_of(g_refs[s], 2 * k + 1 - c, rows), dst_ref=land[s].at[k],
        send_sem=send_sems.at[k * ns + s], recv_sem=recv_sems.at[k * ns + s],
        device_id=(x, y, 1 - c), device_id_type=MESH_ID)
        for k in range(N_CHIPS) for s, (_, rows, _) in enumerate(sections)]


def _pair_exchange_start(sections, grads, barrier_id, name):
    ns = len(sections)

    def body(*refs):
        _pair_handshake(*_position())
        for cp in _pair_copies(sections, refs[:ns], refs[ns:2 * ns], refs[2 * ns], refs[2 * ns + 1]):
            cp.start()
        refs[-1][...] = jnp.zeros_like(refs[-1])

    zones = [lax.empty((N_CHIPS, rows, cols), BF16) for _, rows, cols in sections]
    n = N_CHIPS * ns
    out = pl.pallas_call(
        body, name=name,
        out_shape=(pltpu.SemaphoreType.DMA((n,)), pltpu.SemaphoreType.DMA((n,)),
                   *[pltpu.HBM(a.shape, a.dtype) for a in (*grads, *zones)],
                   jax.ShapeDtypeStruct((8, LANES), F32)),
        in_specs=[HBM] * (2 * ns),
        out_specs=(SEM, SEM, *[HBM] * (2 * ns), pl.BlockSpec(memory_space=pltpu.VMEM)),
        input_output_aliases={i: 2 + i for i in range(2 * ns)},
        compiler_params=pltpu.CompilerParams(has_side_effects=EFFECT, collective_id=barrier_id),
    )(*[_in_hbm(a) for a in grads], *[_in_hbm(a) for a in zones])
    return out[0], out[1], out[2:2 + ns], out[2 + ns:2 + 2 * ns], out[-1]


def _pair_exchange_wait(sections, send_sems, recv_sems, grads, zones, after, name):
    ns = len(sections)

    def body(*refs):
        for cp in _pair_copies(sections, refs[:ns], refs[ns:2 * ns], refs[2 * ns], refs[2 * ns + 1]):
            cp.wait_send()
            cp.wait_recv()

    out = pl.pallas_call(
        body, name=name,
        out_shape=tuple(pltpu.HBM(a.shape, a.dtype) for a in (*grads, *zones)),
        in_specs=[HBM] * (2 * ns) + [SEM, SEM, ANY],
        out_specs=(HBM,) * (2 * ns),
        input_output_aliases={i: i for i in range(2 * ns)},
        compiler_params=pltpu.CompilerParams(has_side_effects=EFFECT),
    )(*grads, *zones, send_sems, recv_sems, after)
    return out[:ns], out[ns:]


def _pair_add(sections, grads, got, core, name):
    ns = len(sections)

    def body(core_ref, *refs):
        g_refs, got_refs, p_refs = refs[:ns], refs[ns:2 * ns], refs[2 * ns:]
        for s in range(ns):
            p_refs[s][0] = (g_refs[s][...].astype(F32) + got_refs[s][0].astype(F32)).astype(BF16)

    slot = [pl.BlockSpec((1, rows, cols), lambda k, c: (k, 0, 0)) for _, rows, cols in sections]
    return pl.pallas_call(
        body, name=name,
        out_shape=tuple(jax.ShapeDtypeStruct((N_CHIPS, rows, cols), BF16) for _, rows, cols in sections),
        grid_spec=pltpu.PrefetchScalarGridSpec(
            num_scalar_prefetch=1, grid=(N_CHIPS,),
            in_specs=[pl.BlockSpec((rows, cols), lambda k, c: (2 * k + c[0], 0)) for _, rows, cols in sections]
            + slot,
            out_specs=tuple(slot)),
        compiler_params=_params(dimension_semantics=("parallel",)),
    )(core, *grads, *got)


def _chip_copies(sections, p_refs, land, send_sems, recv_sems):
    ns = len(sections)
    x, y, c = _position()
    return [pltpu.make_async_remote_copy(
        src_ref=p_refs[s].at[2 * cx + cy], dst_ref=land[s].at[j],
        send_sem=send_sems.at[j * ns + s], recv_sem=recv_sems.at[j * ns + s],
        device_id=(cx, cy, c), device_id_type=MESH_ID)
        for j, (cx, cy) in enumerate(_other_chips(x, y)) for s in range(ns)]


def _chip_exchange(sections, parts, name):
    ns = len(sections)

    def body(*refs):
        copies = _chip_copies(sections, refs[:ns], refs[ns:2 * ns], *refs[2 * ns:])
        for cp in copies:
            cp.start()
        for cp in copies:
            cp.wait_recv()
        for cp in copies:
            cp.wait_send()

    n = 3 * ns
    return pl.pallas_call(
        body, name=name,
        out_shape=tuple(jax.ShapeDtypeStruct((3, rows, cols), BF16) for _, rows, cols in sections),
        in_specs=[ANY] * ns, out_specs=(ANY,) * ns,
        scratch_shapes=[pltpu.SemaphoreType.DMA((n,)), pltpu.SemaphoreType.DMA((n,))],
    )(*parts)


def _chip_exchange_start(sections, parts, name):
    ns = len(sections)

    def body(*refs):
        p_refs, land = refs[:ns], refs[ns:2 * ns]
        send_sems, recv_sems = refs[2 * ns], refs[2 * ns + 1]
        token = refs[-1]
        for cp in _chip_copies(sections, p_refs, land, send_sems, recv_sems):
            cp.start()
        token[...] = jnp.zeros_like(token)

    zones = [lax.empty((3, rows, cols), BF16) for _, rows, cols in sections]
    out = pl.pallas_call(
        body, name=name,
        out_shape=(pltpu.SemaphoreType.DMA((3 * ns,)), pltpu.SemaphoreType.DMA((3 * ns,)),
                   *[pltpu.HBM(a.shape, a.dtype) for a in parts], *[pltpu.HBM(a.shape, a.dtype) for a in zones],
                   jax.ShapeDtypeStruct((8, LANES), F32)),
        in_specs=[HBM] * (2 * ns),
        out_specs=(SEM, SEM, *[HBM] * (2 * ns), pl.BlockSpec(memory_space=pltpu.VMEM)),
        input_output_aliases={i: 2 + i for i in range(2 * ns)},
        compiler_params=pltpu.CompilerParams(has_side_effects=EFFECT),
    )(*[_in_hbm(a) for a in parts], *[_in_hbm(a) for a in zones])
    return out[0], out[1], out[2:2 + ns], out[2 + ns:2 + 2 * ns], out[-1]


def _chip_exchange_wait(sections, send_sems, recv_sems, parts, zones, after, name):
    ns = len(sections)

    def body(*refs):
        p_refs, land = refs[:ns], refs[ns:2 * ns]
        for cp in _chip_copies(sections, p_refs, land, refs[2 * ns], refs[2 * ns + 1]):
            cp.wait_send()
            cp.wait_recv()

    out = pl.pallas_call(
        body, name=name,
        out_shape=tuple(pltpu.HBM(a.shape, a.dtype) for a in (*parts, *zones)),
        in_specs=[HBM] * (2 * ns) + [SEM, SEM, ANY],
        out_specs=(HBM,) * (2 * ns),
        input_output_aliases={i: i for i in range(2 * ns)},
        compiler_params=pltpu.CompilerParams(has_side_effects=EFFECT),
    )(*parts, *zones, send_sems, recv_sems, after)
    return out[:ns], out[ns:]


def _grad_finish(sections, parts, far, chip, name):
    ns = len(sections)

    def body(chip_ref, *refs):
        p_refs, b_refs, g_refs = refs[:ns], refs[ns:2 * ns], refs[2 * ns:]
        for s in range(ns):
            g = p_refs[s][0].astype(F32)
            for j in range(3):
                g = g + b_refs[s][j].astype(F32)
            g_refs[s][...] = g

    half = [(rows // 2, cols) for _, rows, cols in sections]
    return pl.pallas_call(
        body, name=name,
        out_shape=tuple(jax.ShapeDtypeStruct((rows, cols), F32) for _, rows, cols in sections),
        grid_spec=pltpu.PrefetchScalarGridSpec(
            num_scalar_prefetch=1, grid=(2,),
            in_specs=[pl.BlockSpec((1, r, c), lambda i, chip: (chip[0], i, 0)) for r, c in half]
            + [pl.BlockSpec((3, r, c), lambda i, chip: (0, i, 0)) for r, c in half],
            out_specs=tuple(pl.BlockSpec((r, c), lambda i, chip: (i, 0)) for r, c in half)),
        compiler_params=_params(dimension_semantics=("parallel",)),
    )(chip, *parts, *far)


def _sum_devices(parts, rows, name):
    cols = parts.shape[1]
    tr = rows // 2

    def body(*refs):
        s = refs[0][...].astype(F32)
        for d in range(1, N_DEV):
            s = s + refs[d][...].astype(F32)
        refs[N_DEV][...] = s

    return pl.pallas_call(
        body, name=name,
        out_shape=jax.ShapeDtypeStruct((rows, cols), F32),
        grid=(2,),
        in_specs=[pl.BlockSpec((tr, cols), lambda i, d=d: (2 * d + i, 0)) for d in range(N_DEV)],
        out_specs=pl.BlockSpec((tr, cols), lambda i: (i, 0)),
        compiler_params=_params(dimension_semantics=("parallel",)),
    )(*([parts] * N_DEV))


def _adamw_step(w_ref, g_ref, m_ref, v_ref, d_ref, nm_ref, nv_ref):
    c1 = 1.0 / (1.0 - ADAM_B1 ** ADAM_STEP)
    c2 = 1.0 / (1.0 - ADAM_B2 ** ADAM_STEP)
    gv = g_ref[...]
    nm = ADAM_B1 * m_ref[...] + (1.0 - ADAM_B1) * gv
    nv = ADAM_B2 * v_ref[...] + (1.0 - ADAM_B2) * (gv * gv)
    nm_ref[...] = nm
    nv_ref[...] = nv
    d_ref[...] = (-ADAM_LR) * ((nm * c1) / (jnp.sqrt(nv * c2) + ADAM_EPS) + ADAM_WD * w_ref[...])


def _adamw_small(params, name):
    n = len(params)

    def body(*refs):
        for k in range(n):
            _adamw_step(*refs[4 * k:4 * k + 4], *refs[4 * n + 3 * k:4 * n + 3 * k + 3])

    out = pl.pallas_call(
        body, name=name,
        out_shape=tuple(jax.ShapeDtypeStruct(p[0].shape, F32) for p in params for _ in range(3)),
    )(*[a for p in params for a in p])
    return [out[3 * k:3 * k + 3] for k in range(n)]


def _adamw(w, g, m, v, name):
    rows, cols = w.shape
    tr = rows
    while tr * cols * 4 > (1 << 20) and tr % 16 == 0:
        tr //= 2

    def body(*refs):
        _adamw_step(*refs)

    spec = pl.BlockSpec((tr, cols), lambda i: (i, 0))
    shape = jax.ShapeDtypeStruct((rows, cols), F32)
    return pl.pallas_call(
        body, name=name,
        out_shape=(shape, shape, shape),
        grid=(rows // tr,),
        in_specs=[spec] * 4, out_specs=(spec,) * 3,
        compiler_params=_params(dimension_semantics=("parallel",)),
    )(w, g, m, v)


NAMES = ("ln1_g", "w_in", "b_in", "rpb", "w_att_o", "conv_w", "conv_b", "w_rg_a", "b_rg_a", "w_rg_i",
         "b_rg_i", "lru_lambda", "w_rec_o", "w_out", "ln2_g", "w_ff1", "w_ff2", "lnf_g")
TRANSPOSED = {"w_in": "w_in_t", "w_att_o": "w_att_o_t", "w_ff1": "w_ff1_t"}
ROW_SHARDED = ("w_rec_o", "w_out", "w_ff2")
REPLICATED = (("ln1_g", (1, D)), ("b_in", (1, D_IN)), ("rpb", (N_HEADS * N_RPB_R, N_RPB_C)),
              ("conv_b", (1, D_REC)), ("w_rg_a", (2 * N_REC_BLOCKS * REC_BLOCK, REC_BLOCK)),
              ("w_rg_i", (2 * N_REC_BLOCKS * REC_BLOCK, REC_BLOCK)), ("ln2_g", (1, D)), ("lnf_g", (1, D)))
GATE_BLOCKS = ("w_rg_a", "w_rg_i")
SMALL_ROWS = 112


def _chan_bits(vectors):
    chan = jnp.concatenate(vectors, axis=0)
    bits = lax.bitcast_convert_type(chan, BF16).reshape(-1)
    return jnp.pad(bits, (0, CHAN_BLOCK_ROWS * D - bits.shape[0])).reshape(CHAN_BLOCK_ROWS, D)


def _chan_from_bits(gathered):
    bits = gathered.reshape(N_DEV, CHAN_BLOCK_ROWS * D)[:, :2 * N_CHAN_ROWS * LANES]
    chan = lax.bitcast_convert_type(bits.reshape(N_DEV, N_CHAN_ROWS, LANES, 2), F32)
    return chan.transpose(1, 0, 2).reshape(N_CHAN_ROWS, D)


def kernel(x, ln1_g, w_in, b_in, rpb, w_att_o, conv_w, conv_b, w_rg_a, b_rg_a, w_rg_i, b_rg_i, lru_lambda, w_rec_o, w_out, ln2_g, w_ff1, w_ff2, lnf_g, loss_target, m_ln1_g, m_w_in, m_b_in, m_rpb, m_w_att_o, m_conv_w, m_conv_b, m_w_rg_a, m_b_rg_a, m_w_rg_i, m_b_rg_i, m_lru_lambda, m_w_rec_o, m_w_out, m_ln2_g, m_w_ff1, m_w_ff2, m_lnf_g, v_ln1_g, v_w_in, v_b_in, v_rpb, v_w_att_o, v_conv_w, v_conv_b, v_w_rg_a, v_b_rg_a, v_w_rg_i, v_b_rg_i, v_lru_lambda, v_w_rec_o, v_w_out, v_ln2_g, v_w_ff1, v_w_ff2, v_lnf_g):
    w = dict(zip(NAMES, (ln1_g, w_in, b_in, rpb, w_att_o, conv_w, conv_b, w_rg_a, b_rg_a, w_rg_i,
                         b_rg_i, lru_lambda, w_rec_o, w_out, ln2_g, w_ff1, w_ff2, lnf_g)))
    m = dict(zip(NAMES, (m_ln1_g, m_w_in, m_b_in, m_rpb, m_w_att_o, m_conv_w, m_conv_b, m_w_rg_a,
                         m_b_rg_a, m_w_rg_i, m_b_rg_i, m_lru_lambda, m_w_rec_o, m_w_out, m_ln2_g,
                         m_w_ff1, m_w_ff2, m_lnf_g)))
    v = dict(zip(NAMES, (v_ln1_g, v_w_in, v_b_in, v_rpb, v_w_att_o, v_conv_w, v_conv_b, v_w_rg_a,
                         v_b_rg_a, v_w_rg_i, v_b_rg_i, v_lru_lambda, v_w_rec_o, v_w_out, v_ln2_g,
                         v_w_ff1, v_w_ff2, v_lnf_g)))
    xi, yi, ci = _position()

    shard = {t: w[n][0].T.astype(BF16) for n, t in TRANSPOSED.items()}
    shard.update({n: w[n][0].astype(BF16) for n in ROW_SHARDED})
    shard["chan"] = _chan_bits([w[n][0] for n, _ in CHAN])
    first, later = ("w_in_t", "chan"), ("w_rec_o", "w_out", "w_att_o_t", "w_ff1_t", "w_ff2")
    *gathered, done = _all_gather([shard[n] for n in first], "weight_all_gather")
    p = dict(zip(first, gathered))
    send_sems, recv_sems, sent, zones, token = _gather_start([shard[n] for n in later], done,
                                                             "weight_gather_start")

    travelling = {"shards": sent, "zones": zones}
    stages = (("w_rec_o", "w_out", "w_att_o_t"), ("w_ff1_t", "w_ff2"))

    def late_weights(after, stage):
        which = [later.index(n) for n in stages[stage]]
        travelling["shards"], travelling["zones"] = _gather_wait(
            send_sems, recv_sems, travelling["shards"], travelling["zones"], which, after,
            "weight_gather_wait_%d" % stage)
        return dict(zip(stages[stage], _gather_pass_on(
            [shard[n].shape[0] for n in stages[stage]], [travelling["zones"][i] for i in which],
            PASS_ON_IDS[stage], "weight_gather_pass_on_%d" % stage)))

    chan = _chan_from_bits(p.pop("chan"))
    r0 = 0
    for n, rows in CHAN:
        p[n] = chan[r0:r0 + rows]
        r0 += rows
    p.update(ln1_g=w["ln1_g"], b_in=w["b_in"] + token[0, 0], rpb=w["rpb"][0], conv_b=w["conv_b"],
             w_rg_a=w["w_rg_a"][0], w_rg_i=w["w_rg_i"][0], ln2_g=w["ln2_g"],
             lnf_g=w["lnf_g"].reshape(1, D))

    core = jnp.reshape(ci, (1,)).astype(jnp.int32)
    chip = jnp.reshape(2 * xi + yi, (1,)).astype(jnp.int32)
    early_sections, late_sections = SECTIONS[1:], SECTIONS[:1]
    in_flight = {}

    def pair_sum_and_send(group, sections, after):
        send_sems, recv_sems, sect, zones, _ = in_flight["pair_" + group]
        sect, got = _pair_exchange_wait(sections, send_sems, recv_sems, sect, zones, after,
                                        "grad_pair_exchange_wait_" + group)
        parts = _pair_add(sections, sect, got, core, "grad_pair_add_" + group)
        in_flight[group] = _chip_exchange_start(sections, parts, "grad_chip_exchange_start_" + group)
        return in_flight[group][-1]

    def reduce_early(grads):
        chan_g = jnp.concatenate([grads[n] for n, _ in CHAN], axis=0)
        chan_g = chan_g.reshape(N_CHAN_ROWS, N_DEV, LANES).transpose(1, 0, 2).astype(BF16)
        chan_g = jnp.pad(chan_g.reshape(N_DEV, -1), ((0, 0), (0, CHAN_BLOCK_ROWS * D - N_CHAN_ROWS * LANES)))
        grads["chan"] = chan_g.reshape(N_DEV * CHAN_BLOCK_ROWS, D)
        grads["gates"] = jnp.concatenate([grads[n].reshape(-1, D) for n in GATE_BLOCKS], axis=0).astype(BF16)
        in_flight["pair_early"] = _pair_exchange_start(
            early_sections, [grads[n] for n, _, _ in early_sections], PAIR_EARLY_ID,
            "grad_pair_exchange_start_early")
        return pair_sum_and_send("early", early_sections, in_flight["pair_early"][-1])[0, 0]

    loss_part, grad_x, grads = _local_step(x[0], loss_target[0], p, late_weights, reduce_early)
    in_flight["pair_late"] = _pair_exchange_start(
        late_sections, [grads[n] for n, _, _ in late_sections], PAIR_LATE_ID, "grad_pair_exchange_start_late")

    def finish(group, sections, after, name):
        send_sems, recv_sems, parts, zones, _ = in_flight[group]
        parts, far = _chip_exchange_wait(sections, send_sems, recv_sems, parts, zones, after,
                                         "grad_chip_exchange_wait_" + name)
        return dict(zip((n for n, _, _ in sections),
                        _grad_finish(sections, parts, far, chip, "grad_finish_" + name)))

    summed = finish("early", early_sections, in_flight["pair_late"][-1], "early")
    started_late = pair_sum_and_send("late", late_sections, summed["gates"])

    flat = jnp.concatenate([grads[n].reshape(-1) for n, _ in REPLICATED if n not in GATE_BLOCKS]
                           + [loss_part.reshape(-1) + started_late[0, 0]])
    n_small = flat.shape[0]
    flat = jnp.pad(flat, (0, SMALL_ROWS * LANES - n_small)).reshape(SMALL_ROWS, LANES)
    small_parts, gate_sum, _ = _all_gather([flat, summed["gates"]], "small_grad_all_gather")
    small = _sum_devices(small_parts, SMALL_ROWS, "small_grad_sum").reshape(-1)
    loss = small[n_small - 1]

    g, delta, new_m, new_v = {}, {}, {}, {}

    def update(n, g2, shape2):
        d2, m2, v2 = _adamw(w[n].reshape(shape2), g2, m[n].reshape(shape2), v[n].reshape(shape2),
                            "adamw_" + n)
        g[n], delta[n], new_m[n], new_v[n] = (a.reshape(w[n].shape) for a in (g2, d2, m2, v2))

    small_params = []
    o = 0
    for n, shape2 in REPLICATED:
        if n in GATE_BLOCKS:
            k, rows = GATE_BLOCKS.index(n), gate_sum.shape[0] // len(GATE_BLOCKS)
            update(n, gate_sum[k * rows:(k + 1) * rows].reshape(shape2), shape2)
        else:
            size = shape2[0] * shape2[1]
            small_params.append((n, small[o:o + size].reshape(shape2), shape2))
            o += size
    chan_back = summed["chan"].reshape(-1)[:N_CHAN_ROWS * LANES].reshape(N_CHAN_ROWS, LANES)
    r0 = 0
    for n, rows in CHAN:
        small_params.append((n, chan_back[r0:r0 + rows], (rows, LANES)))
        r0 += rows
    results = _adamw_small([(w[n].reshape(s2), g2, m[n].reshape(s2), v[n].reshape(s2))
                            for n, g2, s2 in small_params], "adamw_vectors")
    for (n, g2, _), (d2, m2, v2) in zip(small_params, results):
        g[n], delta[n], new_m[n], new_v[n] = (a.reshape(w[n].shape) for a in (g2, d2, m2, v2))

    for n in ROW_SHARDED:
        update(n, summed[n], summed[n].shape)
    for n, t in TRANSPOSED.items():
        if t in summed:
            update(n, summed[t].T, summed[t].shape[::-1])
    summed = finish("late", late_sections, _after_all(list(delta.values()), "updates_done"), "late")
    update("w_in", summed["w_in_t"].T, summed["w_in_t"].shape[::-1])

    return (loss, grad_x[None], *[g[n] for n in NAMES], *[delta[n] for n in NAMES],
            *[new_m[n] for n in NAMES], *[new_v[n] for n in NAMES])
```

```python
import math

import numpy as np
import jax
import jax.numpy as jnp
from jax import lax
from jax.experimental import pallas as pl
from jax.experimental.pallas import tpu as pltpu

F32 = jnp.float32
BF16 = jnp.bfloat16

T = 2048
D = 1024
D_ATT = 512
D_REC = 1024
D_FF = 4096
D_IN = 5632
N_HEADS = 8
DH = 64
GRID_W = 64
ROWS = T // GRID_W
WIN_H = 8
WIN_W = 16
KWIN = WIN_H * GRID_W
N_RPB_R = 2 * WIN_H - 1
N_RPB_C = 2 * WIN_W - 1
N_REC_BLOCKS = 16
REC_BLOCK = 64
CG = 128
N_CG = D_REC // CG
LRU_C = 8.0
EPS = 1e-6
N_DEV = 8
N_CHIPS = 4
LANES = 128

ADAM_LR = 0.001
ADAM_B1 = 0.9
ADAM_B2 = 0.999
ADAM_EPS = 1e-08
ADAM_WD = 0.01
ADAM_STEP = 10

MESH_AXES = ("x", "y", "c")
VMEM_LIMIT = 56 * 1024 * 1024

TILE = 512
DZ_ARRAYS = ((0, 3, 1), (3, 4, 2), (7, 4, 2))
N_DZ_TILES = D_IN // TILE


def _params(**kw):
    return pltpu.CompilerParams(vmem_limit_bytes=VMEM_LIMIT, **kw)


HG = 4
HQ = HG * GRID_W
HC = HG * DH


def _att_tables():
    rq = np.arange(GRID_W)
    kc = np.arange(KWIN) % GRID_W
    win_start = np.clip(rq - WIN_W // 2, 0, GRID_W - WIN_W)
    valid = (kc[None, :] >= win_start[:, None]) & (kc[None, :] < win_start[:, None] + WIN_W)
    same_head = (np.arange(HQ)[:, None] // GRID_W) == (np.arange(HC)[None, :] // DH)
    return valid.astype(np.float32), same_head.astype(np.float32)


def _pair_mask():
    half = np.arange(2 * DH) // DH
    return (half[:, None] == half[None, :]).astype(np.float32)


def _dup_table():
    return np.concatenate([np.eye(REC_BLOCK, dtype=np.float32)] * 2, axis=1)


def _sigmoid(x):
    return 0.5 * jnp.tanh(0.5 * x) + 0.5


def _softplus(x):
    return jnp.maximum(x, 0.0) + jnp.log(1.0 + jnp.exp(-jnp.abs(x)))


def _one_minus_square(log_a, a):
    x = 2.0 * log_a
    series = -x * (1.0 + x * (0.5 + x * (1.0 / 6.0)))
    return jnp.where(x > -0.02, series, 1.0 - a * a)


_GELU_C = math.sqrt(2.0 / math.pi)


def _gelu_and_grad(x):
    x2 = x * x
    inner = _GELU_C * (x + 0.044715 * x * x2)
    t = jnp.tanh(inner)
    g = 0.5 * x * (1.0 + t)
    dg = 0.5 * (1.0 + t) + 0.5 * x * (1.0 - t * t) * _GELU_C * (1.0 + 3.0 * 0.044715 * x2)
    return g, dg


def _dot(a, b):
    return jnp.dot(a, b, preferred_element_type=F32)


def _dot_nt(a, b):
    return lax.dot_general(a, b, (((1,), (1,)), ((), ())), preferred_element_type=F32)


def _dot_tn(a, b):
    return lax.dot_general(a, b, (((0,), (0,)), ((), ())), preferred_element_type=F32)


def _dot_exact(a, b):
    return jnp.dot(a, b, precision=lax.Precision.HIGHEST, preferred_element_type=F32)


def _shift_rows(x, s):
    n = x.shape[0]
    rows = lax.broadcasted_iota(jnp.int32, x.shape, 0)
    y = pltpu.roll(x, s % n, 0)
    if s > 0:
        return jnp.where(rows >= s, y, 0.0)
    return jnp.where(rows < n + s, y, 0.0)


def _rms_bwd(dh, xh, r, g):
    dxh = dh * g
    return r * (dxh - xh * jnp.mean(dxh * xh, axis=-1, keepdims=True))


def _matmul(a, b, mode, out_dtype, name, tm=512, tn=1024, tk=2048):
    if mode == "nn":
        (m, k), (k2, n) = a.shape, b.shape
    elif mode == "nt":
        (m, k), (n, k2) = a.shape, b.shape
    else:
        (k, m), (k2, n) = a.shape, b.shape
    assert k == k2
    tm, tn, tk = min(tm, m), min(tn, n), min(tk, k)
    assert m % tm == 0 and n % tn == 0 and k % tk == 0
    nk = k // tk
    dot = {"nn": _dot, "nt": _dot_nt, "tn": _dot_tn}[mode]

    def body(a_ref, b_ref, o_ref, acc):
        kk = pl.program_id(2)
        part = dot(a_ref[...].astype(BF16), b_ref[...].astype(BF16))
        if nk == 1:
            o_ref[...] = part.astype(out_dtype)
            return

        @pl.when(kk == 0)
        def _():
            acc[...] = part

        @pl.when(kk > 0)
        def _():
            acc[...] += part

        @pl.when(kk == nk - 1)
        def _():
            o_ref[...] = acc[...].astype(out_dtype)

    if mode == "tn":
        a_spec = pl.BlockSpec((tk, tm), lambda i, j, kk: (kk, i))
    else:
        a_spec = pl.BlockSpec((tm, tk), lambda i, j, kk: (i, kk))
    if mode == "nt":
        b_spec = pl.BlockSpec((tn, tk), lambda i, j, kk: (j, kk))
    else:
        b_spec = pl.BlockSpec((tk, tn), lambda i, j, kk: (kk, j))
    return pl.pallas_call(
        body, name=name,
        out_shape=jax.ShapeDtypeStruct((m, n), out_dtype),
        grid=(m // tm, n // tn, nk),
        in_specs=[a_spec, b_spec],
        out_specs=pl.BlockSpec((tm, tn), lambda i, j, kk: (i, j)),
        scratch_shapes=[pltpu.VMEM((tm, tn) if nk > 1 else (8, LANES), F32)],
        compiler_params=_params(dimension_semantics=("parallel", "parallel", "arbitrary")),
    )(a, b)


def _in_proj(x, g1, w_in_t, b_in):
    tm = 512

    def body(x_ref, g_ref, w_hbm, b_ref, qkv_ref, uy_ref, gg_ref, h_ref, w):
        @pl.when(pl.program_id(0) == 0)
        def _():
            pltpu.sync_copy(w_hbm, w)

        xv = x_ref[...]
        r = lax.rsqrt(jnp.mean(xv * xv, axis=-1, keepdims=True) + EPS)
        h = ((xv * r) * g_ref[...]).astype(BF16)
        h_ref[...] = h
        row0 = 0
        for ref in (qkv_ref, uy_ref, gg_ref):
            for c0 in range(0, ref.shape[1], TILE):
                z = _dot_nt(h, w[row0:row0 + TILE, :]) + b_ref[:, row0:row0 + TILE]
                ref[:, c0:c0 + TILE] = z.astype(ref.dtype)
                row0 += TILE

    tok = lambda width: pl.BlockSpec((tm, width), lambda i: (i, 0))
    return pl.pallas_call(
        body, name="in_proj",
        out_shape=(jax.ShapeDtypeStruct((T, 3 * D_ATT), BF16),
                   jax.ShapeDtypeStruct((T, 2 * D_REC), F32),
                   jax.ShapeDtypeStruct((T, 2 * D), F32),
                   jax.ShapeDtypeStruct((T, D), BF16)),
        grid=(T // tm,),
        in_specs=[tok(D), pl.BlockSpec((1, D), lambda i: (0, 0)), pl.BlockSpec(memory_space=pl.ANY),
                  pl.BlockSpec((1, D_IN), lambda i: (0, 0))],
        out_specs=(tok(3 * D_ATT), tok(2 * D_REC), tok(2 * D), tok(D)),
        scratch_shapes=[pltpu.VMEM((D_IN, D), BF16)],
        compiler_params=_params(dimension_semantics=("arbitrary",)),
    )(x, g1, w_in_t, b_in)


def _dz_specs(rows, tile_of, row_of):
    def spec(off, n, per_plane):
        def index(*ids):
            t = jnp.clip(tile_of(*ids) - off, 0, n - 1)
            return (t // per_plane, row_of(*ids), t % per_plane)
        return pl.BlockSpec((1, rows, TILE), index)
    return [spec(off, n, per) for off, n, per in DZ_ARRAYS]


def _dh_norm1_bwd(dz, w_in_t, x, g1, dx1):
    tm = 512

    def body(dqkv_ref, duy_ref, dgg_ref, w_hbm, x_ref, g_ref, dx1_ref, gx_ref, dg_ref, w):
        @pl.when(pl.program_id(0) == 0)
        def _():
            pltpu.sync_copy(w_hbm, w)
            dg_ref[...] = jnp.zeros_like(dg_ref)

        dh, row0 = None, 0
        for ref in (dqkv_ref, duy_ref, dgg_ref):
            for plane in range(ref.shape[0]):
                cols = ref.shape[2]
                part = _dot(ref[plane], w[row0:row0 + cols, :])
                dh = part if dh is None else dh + part
                row0 += cols
        xv = x_ref[...]
        r = lax.rsqrt(jnp.mean(xv * xv, axis=-1, keepdims=True) + EPS)
        xh = xv * r
        dg_ref[...] += jnp.sum(dh * xh, axis=0, keepdims=True)
        gx_ref[...] = dx1_ref[...] + _rms_bwd(dh, xh, r, g_ref[...])

    tok = pl.BlockSpec((tm, D), lambda i: (i, 0))
    vec = pl.BlockSpec((1, D), lambda i: (0, 0))
    planes = lambda a: pl.BlockSpec((a.shape[0], tm, a.shape[2]), lambda i: (0, i, 0))
    return pl.pallas_call(
        body, name="dh_norm1_bwd",
        out_shape=(jax.ShapeDtypeStruct((T, D), F32), jax.ShapeDtypeStruct((1, D), F32)),
        grid=(T // tm,),
        in_specs=[planes(a) for a in dz] + [pl.BlockSpec(memory_space=pl.ANY), tok, vec, tok],
        out_specs=(tok, vec),
        scratch_shapes=[pltpu.VMEM((D_IN, D), BF16)],
        compiler_params=_params(dimension_semantics=("arbitrary",)),
    )(*dz, w_in_t, x, g1, dx1)


def _grad_w_in(dz, h):
    def body(*refs):
        seg_refs = refs[:3]
        h_ref, gw_ref, gb_ref = refs[3:]
        j = pl.program_id(0)

        for s, (off, n, _) in enumerate(DZ_ARRAYS):
            @pl.when((j >= off) & (j < off + n))
            def _(s=s):
                a = seg_refs[s][0]
                gw_ref[...] = _dot_tn(a, h_ref[...]).astype(BF16)
                gb_ref[...] = jnp.sum(a.astype(F32), axis=0, keepdims=True)

    return pl.pallas_call(
        body, name="grad_w_in",
        out_shape=(jax.ShapeDtypeStruct((D_IN, D), BF16), jax.ShapeDtypeStruct((1, D_IN), F32)),
        grid=(N_DZ_TILES,),
        in_specs=_dz_specs(T, lambda j: j, lambda j: 0) + [pl.BlockSpec((T, D), lambda j: (0, 0))],
        out_specs=(pl.BlockSpec((TILE, D), lambda j: (j, 0)), pl.BlockSpec((1, TILE), lambda j: (0, j))),
        compiler_params=_params(dimension_semantics=("parallel",)),
    )(*dz, h)


def _rpb_rows(rpb):
    padded = jnp.pad(rpb, ((0, 0), (0, 0), (0, GRID_W - N_RPB_C)))
    rows = [padded[:, WIN_H - 1 - oi: 2 * WIN_H - 1 - oi].reshape(N_HEADS // HG, HG, KWIN)
            for oi in range(WIN_H)]
    return jnp.stack(rows, axis=0)


SKEW = KWIN - (WIN_W - 1)


MASKED = -1e30


def _bias_tiles(rows_ref, valid, bias_s):
    for oi in range(WIN_H):
        for hh in range(HG):
            row = jnp.broadcast_to(rows_ref[oi, 0, hh:hh + 1, :], (GRID_W, KWIN))
            tile = pltpu.roll(row, SKEW, 1, stride=1, stride_axis=0)
            bias_s[oi, hh * GRID_W:(hh + 1) * GRID_W, :] = jnp.where(valid, tile, MASKED)


def _bias_tile_grads(gb_s, flip, out_ref):
    for oi in range(WIN_H):
        for hh in range(HG):
            g = _dot_exact(flip, gb_s[oi, hh * GRID_W:(hh + 1) * GRID_W, :])
            back = pltpu.roll(g, KWIN - (GRID_W - WIN_W), 1, stride=1, stride_axis=0)
            out_ref[0, oi, hh:hh + 1, :] = jnp.sum(back, axis=0, keepdims=True)


def _rpb_fold(row_grads):
    g = row_grads.transpose(1, 0, 2, 3).reshape(WIN_H, N_HEADS, WIN_H, GRID_W)
    g = g.transpose(0, 2, 1, 3)

    def body(g_ref, o_ref):
        for dr in range(N_RPB_R):
            terms = [g_ref[oi, i] for oi in range(WIN_H) for i in range(WIN_H) if i - oi + WIN_H - 1 == dr]
            acc = terms[0]
            for term in terms[1:]:
                acc = acc + term
            o_ref[dr] = acc

    out = pl.pallas_call(
        body, name="rpb_fold",
        out_shape=jax.ShapeDtypeStruct((N_RPB_R, N_HEADS, GRID_W), F32),
    )(g)
    return out.transpose(1, 0, 2)[:, :, :N_RPB_C]


ATT_GROUPS = N_HEADS // HG
ATT_UNROLL = 8


def _stacked(rows64, same_head):
    return jnp.where(same_head, jnp.concatenate([rows64] * HG, axis=0), jnp.zeros((), BF16))


def _own_heads(stacked):
    head = lax.broadcasted_iota(jnp.int32, (GRID_W, HC), 1) // DH
    out = stacked[:GRID_W]
    for h in range(1, HG):
        out = jnp.where(head == h, stacked[h * GRID_W:(h + 1) * GRID_W], out)
    return out


def _att_scores(q_ref, k_ref, bias_ref, same_head, r):
    rs = jnp.clip(r - WIN_H // 2, 0, ROWS - WIN_H)
    oi = r - rs
    q0 = pl.multiple_of(r * GRID_W, GRID_W)
    k0 = pl.multiple_of(rs * GRID_W, GRID_W)
    q2 = _stacked(q_ref[pl.ds(q0, GRID_W), :] * (DH ** -0.5), same_head)
    kw = k_ref[pl.ds(k0, KWIN), :]
    s = _dot_nt(q2, kw) + bias_ref[oi]
    e = jnp.exp(s - jnp.max(s, axis=-1, keepdims=True))
    return e, 1.0 / jnp.sum(e, axis=-1, keepdims=True), q2, kw, q0, k0, oi


def _att_specs():
    col = lambda off: pl.BlockSpec((T, HC), lambda g: (0, g + off * ATT_GROUPS))
    tables = [pl.BlockSpec((WIN_H, 1, HG, KWIN), lambda g: (0, g, 0, 0)),
              pl.BlockSpec((GRID_W, KWIN), lambda g: (0, 0)),
              pl.BlockSpec((HQ, HC), lambda g: (0, 0))]
    return col, tables, pltpu.VMEM((WIN_H, HQ, KWIN), F32)


def _att_fwd(qkv, bias_rows):
    valid_np, same_head_np = _att_tables()

    def body(q_ref, k_ref, v_ref, rows_ref, valid_ref, head_ref, o_ref, bias_s):
        same_head = head_ref[...] > 0.5
        _bias_tiles(rows_ref, valid_ref[...] > 0.5, bias_s)

        def row(r, carry):
            e, rl, _, _, q0, k0, _ = _att_scores(q_ref, k_ref, bias_s, same_head, r)
            o2 = _dot((e * rl).astype(BF16), v_ref[pl.ds(k0, KWIN), :])
            o_ref[pl.ds(q0, GRID_W), :] = _own_heads(o2).astype(BF16)
            return carry

        lax.fori_loop(0, ROWS, row, 0, unroll=ATT_UNROLL)

    col, tables, tiles = _att_specs()
    return pl.pallas_call(
        body, name="att_fwd",
        out_shape=jax.ShapeDtypeStruct((T, D_ATT), BF16),
        grid=(ATT_GROUPS,),
        in_specs=[col(0), col(1), col(2)] + tables,
        out_specs=col(0),
        scratch_shapes=[tiles],
        compiler_params=_params(dimension_semantics=("parallel",)),
    )(qkv, qkv, qkv, bias_rows, jnp.asarray(valid_np), jnp.asarray(same_head_np))


def _att_bwd(qkv, bias_rows, datt, after):
    valid_np, same_head_np = _att_tables()

    def body(q_ref, k_ref, v_ref, do_ref, rows_ref, valid_ref, head_ref, flip_ref,
             dqkv_ref, grows_ref, dk_acc, dv_acc, bias_s, gb_s):
        same_head = head_ref[...] > 0.5
        dk_acc[...] = jnp.zeros_like(dk_acc)
        dv_acc[...] = jnp.zeros_like(dv_acc)
        gb_s[...] = jnp.zeros_like(gb_s)
        _bias_tiles(rows_ref, valid_ref[...] > 0.5, bias_s)

        def row(r, carry):
            e, rl, q2, kw, q0, k0, oi = _att_scores(q_ref, k_ref, bias_s, same_head, r)
            do2 = _stacked(do_ref[pl.ds(q0, GRID_W), :], same_head)
            vw = v_ref[pl.ds(k0, KWIN), :]
            p = e * rl
            dp = _dot_nt(do2, vw)
            ds = p * (dp - jnp.sum(dp * p, axis=-1, keepdims=True))
            p16 = p.astype(BF16)
            ds16 = ds.astype(BF16)
            dv_acc[pl.ds(k0, KWIN), :] += _dot_tn(p16, do2)
            dk_acc[pl.ds(k0, KWIN), :] += _dot_tn(ds16, q2)
            dq2 = _dot(ds16, kw) * (DH ** -0.5)
            dqkv_ref[0, pl.ds(q0, GRID_W), :] = _own_heads(dq2).astype(BF16)
            gb_s[oi] += ds
            return carry

        lax.fori_loop(0, ROWS, row, 0, unroll=ATT_UNROLL)
        dqkv_ref[1] = dk_acc[...].astype(BF16)
        dqkv_ref[2] = dv_acc[...].astype(BF16)
        _bias_tile_grads(gb_s, flip_ref[...], grows_ref)

    col, tables, tiles = _att_specs()
    return pl.pallas_call(
        body, name="att_bwd",
        out_shape=(jax.ShapeDtypeStruct((3, T, D_ATT), BF16),
                   jax.ShapeDtypeStruct((ATT_GROUPS, WIN_H, HG, KWIN), F32)),
        grid=(ATT_GROUPS,),
        in_specs=[col(0), col(1), col(2), col(0)] + tables + [pl.BlockSpec((GRID_W, GRID_W), lambda g: (0, 0))],
        out_specs=(pl.BlockSpec((3, T, HC), lambda g: (0, 0, g)),
                   pl.BlockSpec((1, WIN_H, HG, KWIN), lambda g: (g, 0, 0, 0))),
        scratch_shapes=[pltpu.VMEM((T, HC), F32), pltpu.VMEM((T, HC), F32), tiles, tiles],
        compiler_params=_params(dimension_semantics=("parallel",)),
    )(qkv, qkv, qkv, datt, bias_rows, jnp.asarray(valid_np) + after, jnp.asarray(same_head_np),
      jnp.asarray(np.eye(GRID_W, dtype=np.float32)[::-1].copy()))


def _conv_taps(up):
    return (_shift_rows(up, 2), _shift_rows(up, 1), up, _shift_rows(up, -1))


def _pair_block_diag(w_pair, dup, same_half):
    return jnp.where(same_half, _dot(w_pair.astype(BF16), dup), 0.0).astype(BF16)


def _gates(u, u16, wa, ba, wi, bi, lam):
    r = _sigmoid(_dot(u16, wa) + ba)
    ig = _sigmoid(_dot(u16, wi) + bi)
    sp = _softplus(-lam)
    log_a = (-LRU_C) * r * sp
    a = jnp.exp(log_a)
    mult2 = jnp.maximum(_one_minus_square(log_a, a), 0.0)
    return r, ig, sp, a, jnp.sqrt(mult2), mult2


SCAN_BLOCKS = 8


def _scans(jobs):
    c = jobs[0][0].shape[1]
    nblk = T // 8
    rows = lax.broadcasted_iota(jnp.int32, (8, c), 0)

    def block(a, b, reverse):
        for s in (1, 2, 4):
            if reverse:
                keep = rows < 8 - s
                a_s = jnp.where(keep, pltpu.roll(a, 8 - s, 0), 1.0)
                b_s = jnp.where(keep, pltpu.roll(b, 8 - s, 0), 0.0)
            else:
                keep = rows >= s
                a_s = jnp.where(keep, pltpu.roll(a, s, 0), 1.0)
                b_s = jnp.where(keep, pltpu.roll(b, s, 0), 0.0)
            b = a * b_s + b
            a = a * a_s
        return a, b

    def step(i, carry):
        out = []
        for (a_ref, b_ref, h_ref, reverse), h_prev in zip(jobs, carry):
            for u in range(SCAN_BLOCKS):
                blk = i * SCAN_BLOCKS + u
                if reverse:
                    blk = nblk - 1 - blk
                t0 = pl.multiple_of(blk * 8, 8)
                a, b = block(a_ref[pl.ds(t0, 8), :], b_ref[pl.ds(t0, 8), :], reverse)
                h = a * h_prev + b
                h_ref[pl.ds(t0, 8), :] = h
                h_prev = jnp.broadcast_to(h[0:1] if reverse else h[7:8], (8, c))
            out.append(h_prev)
        return tuple(out)

    lax.fori_loop(0, nblk // SCAN_BLOCKS, step, tuple(jnp.zeros((8, c), F32) for _ in jobs))


def _rec_specs():
    tok = lambda off: pl.BlockSpec((T, CG), lambda g: (0, g + off))
    per_ch = lambda rows: pl.BlockSpec((rows, CG), lambda g: (0, g))
    wspec = pl.BlockSpec((2, 1, CG, REC_BLOCK), lambda g: (0, g, 0, 0))
    const = lambda shape: pl.BlockSpec(shape, lambda g: (0, 0))
    return tok, per_ch, wspec, const


def _rec_fwd(uy, conv_w, conv_b, w_a, b_a, w_i, b_i, lam):
    tok, per_ch, wspec, const = _rec_specs()

    def body(up_ref, yb_ref, cw_ref, cb_ref, wa_ref, ba_ref, wi_ref, bi_ref, lam_ref, dup_ref, half_ref,
             hf_ref, hb_ref, yrec_ref, am_ref, bx_f, bx_b):
        dup = dup_ref[...]
        same_half = half_ref[...] > 0.5
        taps = _conv_taps(up_ref[...])
        u = cb_ref[...]
        for j in range(4):
            u = u + taps[j] * cw_ref[j:j + 1, :]
        u16 = u.astype(BF16)
        for d, bx_s in enumerate((bx_f, bx_b)):
            wa = _pair_block_diag(wa_ref[d, 0], dup, same_half)
            wi = _pair_block_diag(wi_ref[d, 0], dup, same_half)
            _, ig, _, a, mult, _ = _gates(u, u16, wa, ba_ref[d:d + 1, :], wi, bi_ref[d:d + 1, :],
                                       lam_ref[d:d + 1, :])
            am_ref[2 * d] = a
            am_ref[2 * d + 1] = mult
            bx_s[...] = mult * (ig * u)
        _scans([(am_ref.at[0], bx_f, hf_ref, False), (am_ref.at[2], bx_b, hb_ref, True)])
        gelu, _ = _gelu_and_grad(yb_ref[...])
        yrec_ref[...] = ((hf_ref[...] + hb_ref[...]) * gelu).astype(BF16)

    return pl.pallas_call(
        body, name="rec_fwd",
        out_shape=(jax.ShapeDtypeStruct((T, D_REC), F32), jax.ShapeDtypeStruct((T, D_REC), F32),
                   jax.ShapeDtypeStruct((T, D_REC), BF16), jax.ShapeDtypeStruct((4, T, D_REC), F32)),
        grid=(N_CG,),
        in_specs=[tok(0), tok(N_CG), per_ch(4), per_ch(1), wspec, per_ch(2), wspec, per_ch(2), per_ch(2),
                  const((REC_BLOCK, CG)), const((CG, CG))],
        out_specs=(tok(0), tok(0), tok(0), pl.BlockSpec((4, T, CG), lambda g: (0, 0, g))),
        scratch_shapes=[pltpu.VMEM((T, CG), F32)] * 2,
        compiler_params=_params(dimension_semantics=("parallel",)),
    )(uy, uy, conv_w, conv_b, w_a, b_a, w_i, b_i, lam,
      jnp.asarray(_dup_table(), BF16), jnp.asarray(_pair_mask()))


def _rec_bwd(uy, hf, hb, am, dyrec, conv_w, conv_b, w_a, b_a, w_i, b_i, lam):
    tok, per_ch, wspec, const = _rec_specs()

    def body(up_ref, yb_ref, hf_ref, hb_ref, am_ref, dy_ref, cw_ref, cb_ref, wa_ref, ba_ref, wi_ref, bi_ref,
             lam_ref, dup_ref, dupt_ref, half_ref,
             duy_ref, dcw_ref, dcb_ref, dwa_ref, dba_ref, dwi_ref, dbi_ref, dlam_ref,
             a_s0, a_s1, dh_s, g_s0, g_s1):
        dup = dup_ref[...]
        dup_t = dupt_ref[...]
        same_half = half_ref[...] > 0.5
        taps = _conv_taps(up_ref[...])
        u = cb_ref[...]
        for j in range(4):
            u = u + taps[j] * cw_ref[j:j + 1, :]
        u16 = u.astype(BF16)
        gelu, dgelu = _gelu_and_grad(yb_ref[...])
        dy = dy_ref[...]
        duy_ref[1] = (dy * (hf_ref[...] + hb_ref[...]) * dgelu).astype(BF16)
        dh_s[...] = dy * gelu
        a_s0[...] = _shift_rows(am_ref[0], -1)
        a_s1[...] = _shift_rows(am_ref[2], 1)
        _scans([(a_s0, dh_s, g_s0, True), (a_s1, dh_s, g_s1, False)])
        du = jnp.zeros((T, CG), F32)
        for d, g_s in enumerate((g_s0, g_s1)):
            reverse = d == 1
            wa = _pair_block_diag(wa_ref[d, 0], dup, same_half)
            wi = _pair_block_diag(wi_ref[d, 0], dup, same_half)
            lam_d = lam_ref[d:d + 1, :]
            r = _sigmoid(_dot(u16, wa) + ba_ref[d:d + 1, :])
            ig = _sigmoid(_dot(u16, wi) + bi_ref[d:d + 1, :])
            sp = _softplus(-lam_d)
            a, mult = am_ref[2 * d], am_ref[2 * d + 1]
            mult2 = mult * mult
            g = g_s[...]
            h_prev = _shift_rows(hb_ref[...], -1) if reverse else _shift_rows(hf_ref[...], 1)
            da = g * h_prev
            dmult = g * (ig * u)
            dig = g * mult * u
            du = du + g * mult * ig
            dmult_dlog = jnp.where(mult2 > 0.0, -(a * a) * lax.rsqrt(mult2), 0.0)
            dlog_a = da * a + dmult * dmult_dlog
            dr = dlog_a * ((-LRU_C) * sp)
            dsp = jnp.sum(dlog_a * ((-LRU_C) * r), axis=0, keepdims=True)
            dlam_ref[d:d + 1, :] = dsp * (-_sigmoid(-lam_d))
            dga = dr * r * (1.0 - r)
            dgi = dig * ig * (1.0 - ig)
            dga16 = dga.astype(BF16)
            dgi16 = dgi.astype(BF16)
            du = du + _dot_nt(dga16, wa) + _dot_nt(dgi16, wi)
            dwa_ref[d, 0] = _dot_exact(jnp.where(same_half, _dot_tn(u16, dga16), 0.0), dup_t)
            dwi_ref[d, 0] = _dot_exact(jnp.where(same_half, _dot_tn(u16, dgi16), 0.0), dup_t)
            dba_ref[d:d + 1, :] = jnp.sum(dga, axis=0, keepdims=True)
            dbi_ref[d:d + 1, :] = jnp.sum(dgi, axis=0, keepdims=True)
        dcb_ref[...] = jnp.sum(du, axis=0, keepdims=True)
        for j in range(4):
            dcw_ref[j:j + 1, :] = jnp.sum(du * taps[j], axis=0, keepdims=True)
        dup_in = (_shift_rows(du, -2) * cw_ref[0:1, :] + _shift_rows(du, -1) * cw_ref[1:2, :]
                  + du * cw_ref[2:3, :] + _shift_rows(du, 1) * cw_ref[3:4, :])
        duy_ref[0] = dup_in.astype(BF16)

    wshape = jax.ShapeDtypeStruct((2, N_CG, CG, REC_BLOCK), F32)
    vec = lambda rows: jax.ShapeDtypeStruct((rows, D_REC), F32)
    dup_np = _dup_table()
    return pl.pallas_call(
        body, name="rec_bwd",
        out_shape=(jax.ShapeDtypeStruct((2, T, D_REC), BF16),
                   vec(4), vec(1), wshape, vec(2), wshape, vec(2), vec(2)),
        grid=(N_CG,),
        in_specs=[tok(0), tok(N_CG), tok(0), tok(0), pl.BlockSpec((4, T, CG), lambda g: (0, 0, g)), tok(0),
                  per_ch(4), per_ch(1), wspec, per_ch(2), wspec, per_ch(2), per_ch(2),
                  const((REC_BLOCK, CG)), const((CG, REC_BLOCK)), const((CG, CG))],
        out_specs=(pl.BlockSpec((2, T, CG), lambda g: (0, 0, g)),
                   per_ch(4), per_ch(1), wspec, per_ch(2), wspec, per_ch(2), per_ch(2)),
        scratch_shapes=[pltpu.VMEM((T, CG), F32)] * 5,
        compiler_params=_params(dimension_semantics=("parallel",)),
    )(uy, uy, hf, hb, am, dyrec, conv_w, conv_b, w_a, b_a, w_i, b_i, lam,
      jnp.asarray(dup_np, BF16), jnp.asarray(dup_np.T.copy()), jnp.asarray(_pair_mask()))


TM_MIX = 256


def _mix_specs():
    tok = lambda width, blk=0: pl.BlockSpec((TM_MIX, width), lambda i: (i, blk))
    full = lambda shape: pl.BlockSpec(shape, lambda i: (0, 0))
    return tok, full


def _mix_fwd(x, att, yrec, gg, w_att_o_t, w_rec_o, w_out):
    tok, full = _mix_specs()

    def body(x_ref, att_ref, yr_ref, ga_ref, gr_ref, wao_ref, wro_ref, wo_ref, x1_ref, mixed_ref):
        y_att = _dot_nt(att_ref[...], wao_ref[...])
        y_rec = _dot(yr_ref[...], wro_ref[...])
        mixed = (_sigmoid(ga_ref[...]) * y_att + _sigmoid(gr_ref[...]) * y_rec).astype(BF16)
        mixed_ref[...] = mixed
        x1_ref[...] = x_ref[...] + _dot(mixed, wo_ref[...])

    return pl.pallas_call(
        body, name="mix_fwd",
        out_shape=(jax.ShapeDtypeStruct((T, D), F32), jax.ShapeDtypeStruct((T, D), BF16)),
        grid=(T // TM_MIX,),
        in_specs=[tok(D), tok(D_ATT), tok(D_REC), tok(D, 0), tok(D, 1),
                  full((D, D_ATT)), full((D_REC, D)), full((D, D))],
        out_specs=(tok(D), tok(D)),
        compiler_params=_params(dimension_semantics=("parallel",)),
    )(x, att, yrec, gg, gg, w_att_o_t, w_rec_o, w_out)


def _mix_bwd(dx1, att, yrec, gg, w_att_o_t, w_rec_o, w_out, after):
    tok, full = _mix_specs()

    def body(dx_ref, att_ref, yr_ref, ga_ref, gr_ref, wao_ref, wro_ref, wo_ref, after_ref,
             dgg_ref, dya_ref, dyr_ref, datt_ref, dyrp_ref):
        dmixed = _dot_nt(dx_ref[...].astype(BF16), wo_ref[...])
        y_att = _dot_nt(att_ref[...], wao_ref[...])
        y_rec = _dot(yr_ref[...], wro_ref[...])
        sa = _sigmoid(ga_ref[...])
        sr = _sigmoid(gr_ref[...])
        dgg_ref[0] = (dmixed * y_att * sa * (1.0 - sa)).astype(BF16)
        dgg_ref[1] = (dmixed * y_rec * sr * (1.0 - sr)).astype(BF16)
        dya = (dmixed * sa).astype(BF16)
        dyr = (dmixed * sr).astype(BF16)
        dya_ref[...] = dya
        dyr_ref[...] = dyr
        datt_ref[...] = _dot(dya, wao_ref[...]).astype(BF16)
        dyrp_ref[...] = _dot_nt(dyr, wro_ref[...])

    return pl.pallas_call(
        body, name="mix_bwd",
        out_shape=(jax.ShapeDtypeStruct((2, T, D), BF16),
                   jax.ShapeDtypeStruct((T, D), BF16), jax.ShapeDtypeStruct((T, D), BF16),
                   jax.ShapeDtypeStruct((T, D_ATT), BF16), jax.ShapeDtypeStruct((T, D_REC), F32)),
        grid=(T // TM_MIX,),
        in_specs=[tok(D), tok(D_ATT), tok(D_REC), tok(D, 0), tok(D, 1),
                  full((D, D_ATT)), full((D_REC, D)), full((D, D)), pl.BlockSpec(memory_space=pl.ANY)],
        out_specs=(pl.BlockSpec((2, TM_MIX, D), lambda i: (0, i, 0)),
                   tok(D), tok(D), tok(D_ATT), tok(D_REC)),
        compiler_params=_params(dimension_semantics=("parallel",)),
    )(dx1, att, yrec, gg, gg, w_att_o_t, w_rec_o, w_out, after)


TM_FFN = 256
FF_CHUNK = 1024


def _ffn_loss(x1, target, g2, gf, w_ff1_t, w_ff2):
    n_chunks = D_FF // FF_CHUNK

    def body(x1_ref, tg_ref, g2_ref, gf_ref, w1_hbm, w2_hbm,
             loss_ref, dx1_ref, h2_ref, act_ref, dpre_ref, dx2_ref, dg2_ref, dgf_ref,
             w1, w2, relu_s):
        i = pl.program_id(0)

        @pl.when(i == 0)
        def _():
            pltpu.sync_copy(w1_hbm, w1)
            pltpu.sync_copy(w2_hbm, w2)
            loss_ref[...] = jnp.zeros_like(loss_ref)
            dg2_ref[...] = jnp.zeros_like(dg2_ref)
            dgf_ref[...] = jnp.zeros_like(dgf_ref)

        x1v = x1_ref[...]
        r2 = lax.rsqrt(jnp.mean(x1v * x1v, axis=-1, keepdims=True) + EPS)
        xh2 = x1v * r2
        h2 = (xh2 * g2_ref[...]).astype(BF16)
        h2_ref[...] = h2
        x2 = x1v
        for c in range(n_chunks):
            ff = slice(c * FF_CHUNK, (c + 1) * FF_CHUNK)
            rl = jnp.maximum(_dot_nt(h2, w1[ff, :]), 0.0)
            relu_s[:, ff] = rl
            act = (rl * rl).astype(BF16)
            act_ref[:, ff] = act
            x2 = x2 + _dot(act, w2[ff, :])
        r3 = lax.rsqrt(jnp.mean(x2 * x2, axis=-1, keepdims=True) + EPS)
        xh3 = x2 * r3
        err = xh3 * gf_ref[...] - tg_ref[...]
        loss_ref[...] += 0.5 * jnp.sum(jnp.mean(err * err, axis=-1, keepdims=True))
        dy = err * (1.0 / D)
        dgf_ref[...] += jnp.sum(dy * xh3, axis=0, keepdims=True)
        dx2 = _rms_bwd(dy, xh3, r3, gf_ref[...])
        dx2_16 = dx2.astype(BF16)
        dx2_ref[...] = dx2_16
        dh2 = jnp.zeros((TM_FFN, D), F32)
        for c in range(n_chunks):
            ff = slice(c * FF_CHUNK, (c + 1) * FF_CHUNK)
            dpre = (_dot_nt(dx2_16, w2[ff, :]) * (2.0 * relu_s[:, ff])).astype(BF16)
            dpre_ref[:, ff] = dpre
            dh2 = dh2 + _dot(dpre, w1[ff, :])
        dg2_ref[...] += jnp.sum(dh2 * xh2, axis=0, keepdims=True)
        dx1_ref[...] = dx2 + _rms_bwd(dh2, xh2, r2, g2_ref[...])

    tok = lambda width: pl.BlockSpec((TM_FFN, width), lambda i: (i, 0))
    vec = pl.BlockSpec((1, D), lambda i: (0, 0))
    hbm = pl.BlockSpec(memory_space=pl.ANY)
    return pl.pallas_call(
        body, name="ffn_loss",
        out_shape=(jax.ShapeDtypeStruct((8, 128), F32), jax.ShapeDtypeStruct((T, D), F32),
                   jax.ShapeDtypeStruct((T, D), BF16), jax.ShapeDtypeStruct((T, D_FF), BF16),
                   jax.ShapeDtypeStruct((T, D_FF), BF16), jax.ShapeDtypeStruct((T, D), BF16),
                   jax.ShapeDtypeStruct((1, D), F32), jax.ShapeDtypeStruct((1, D), F32)),
        grid=(T // TM_FFN,),
        in_specs=[tok(D), tok(D), vec, vec, hbm, hbm],
        out_specs=(pl.BlockSpec((8, 128), lambda i: (0, 0)), tok(D), tok(D), tok(D_FF), tok(D_FF), tok(D),
                   vec, vec),
        scratch_shapes=[pltpu.VMEM((D_FF, D), BF16), pltpu.VMEM((D_FF, D), BF16),
                        pltpu.VMEM((TM_FFN, D_FF), F32)],
        compiler_params=_params(dimension_semantics=("arbitrary",)),
    )(x1, target, g2, gf, w_ff1_t, w_ff2)


def _local_step(x, target, p, late_weights, reduce_first, reduce_early):
    bias = _rpb_rows(p["rpb"])
    pairs = lambda w: w.reshape(2, N_CG, CG, REC_BLOCK)
    w_a, w_i = pairs(p["w_rg_a"]), pairs(p["w_rg_i"])
    rec_params = (p["conv_w"], p["conv_b"], w_a, p["b_rg_a"], w_i, p["b_rg_i"], p["lru_lambda"])

    qkv, uy, gg, h = _in_proj(x, p["ln1_g"], p["w_in_t"], p["b_in"])
    att = _att_fwd(qkv, bias)
    hf, hb, yrec, am = _rec_fwd(uy, *rec_params)
    p = {**p, **late_weights(yrec, 0)}
    x1, mixed = _mix_fwd(x, att, yrec, gg, p["w_att_o_t"], p["w_rec_o"], p["w_out"])
    p = {**p, **late_weights(x1, 1)}
    loss8, dx1, h2, act, dpre, dx2, g_ln2, g_lnf = _ffn_loss(
        x1, target, p["ln2_g"], p["lnf_g"], p["w_ff1_t"], p["w_ff2"])

    grads = {"ln2_g": g_ln2, "lnf_g": g_lnf,
             "w_ff1_t": _matmul(dpre, h2, "tn", BF16, "g_w_ff1"),
             "w_ff2": _matmul(act, dx2, "tn", BF16, "g_w_ff2")}
    dgg, dya, dyr, datt, dyrp = _mix_bwd(dx1, att, yrec, gg, p["w_att_o_t"], p["w_rec_o"], p["w_out"],
                                         reduce_first(grads))
    duy, g_cw, g_cb, g_wa, g_ba, g_wi, g_bi, g_lam = _rec_bwd(uy, hf, hb, am, dyrp, *rec_params)
    blocks = lambda g: g.reshape(2, N_REC_BLOCKS, REC_BLOCK, REC_BLOCK)
    grads.update({
        "w_att_o_t": _matmul(dya, att, "tn", BF16, "g_w_att_o"),
        "conv_w": g_cw, "conv_b": g_cb, "w_rg_a": blocks(g_wa), "b_rg_a": g_ba,
        "w_rg_i": blocks(g_wi), "b_rg_i": g_bi, "lru_lambda": g_lam,
        "w_rec_o": _matmul(yrec, dyr, "tn", BF16, "g_w_rec_o"),
        "w_out": _matmul(mixed, dx1, "tn", BF16, "g_w_out"),
    })
    dqkv, gbias = _att_bwd(qkv, bias, datt, reduce_early(grads))
    dz = (dqkv, duy, dgg)
    grad_x, g_ln1 = _dh_norm1_bwd(dz, p["w_in_t"], x, p["ln1_g"], dx1)
    g_w_in_t, g_b_in = _grad_w_in(dz, h)
    grads.update(ln1_g=g_ln1, w_in_t=g_w_in_t, b_in=g_b_in, rpb=_rpb_fold(gbias))
    return loss8[0:1, 0:1], grad_x, grads


MESH_ID = pl.DeviceIdType.MESH
ANY = pl.BlockSpec(memory_space=pl.ANY)

CHAN_BLOCK_ROWS = 32
GATE_ROWS = 2 * 2 * N_REC_BLOCKS * REC_BLOCK * REC_BLOCK // (N_DEV * D)
SECTIONS = (("w_in_t", 704, D), ("w_rec_o", 128, D), ("w_out", 128, D), ("w_ff1_t", 512, D),
            ("w_ff2", 512, D), ("chan", CHAN_BLOCK_ROWS, D), ("w_att_o_t", 128, D_ATT),
            ("gates", GATE_ROWS, D))
N_SEC = len(SECTIONS)
N_CHAN_ROWS = 10
CHAN = (("conv_w", 4), ("b_rg_a", 2), ("b_rg_i", 2), ("lru_lambda", 2))


def _position():
    return lax.axis_index("x"), lax.axis_index("y"), lax.axis_index("c")


def _other_chips(x, y):
    return [(1 - x, y), (x, 1 - y), (1 - x, 1 - y)]


PASS_ON_IDS, PAIR_EARLY_ID, PAIR_LATE_ID, PAIR_FIRST_ID = (1, 4), 2, 3, 5


def _pair_handshake(x, y, c):
    barrier = pltpu.get_barrier_semaphore()
    pl.semaphore_signal(barrier, inc=1, device_id=(x, y, 1 - c), device_id_type=MESH_ID)
    pl.semaphore_wait(barrier, 1)


def _block_of(ref, dev, rows):
    return ref.at[pl.ds(pl.multiple_of(dev * rows, 16), rows)]


def _all_gather(shards, name):
    ns = len(shards)

    def body(*refs):
        x_refs, out_refs, done_ref = refs[:ns], refs[ns:2 * ns], refs[2 * ns]
        send_sems, recv_sems, local_sems = refs[2 * ns + 1:]
        done_ref[0, 0] = 0.0
        x, y, c = _position()
        me, sibling = (x, y, c), (x, y, 1 - c)
        x_nbr, y_nbr, diagonal = _other_chips(x, y)
        north = c == 1
        relay_from = (jnp.where(north, x_nbr[0], y_nbr[0]), jnp.where(north, x_nbr[1], y_nbr[1]))
        relay_to = (jnp.where(north, y_nbr[0], x_nbr[0]), jnp.where(north, y_nbr[1], x_nbr[1]))

        def rows(s, px, py, pc):
            return _block_of(out_refs[s], 4 * px + 2 * py + pc, shards[s].shape[0])

        def copy(k, s, block, to, from_shard=False):
            return pltpu.make_async_remote_copy(
                src_ref=x_refs[s] if from_shard else rows(s, *block), dst_ref=rows(s, *block),
                send_sem=send_sems.at[k * ns + s], recv_sem=recv_sems.at[k * ns + s],
                device_id=to, device_id_type=MESH_ID)

        sections = range(ns)
        mine = [pltpu.make_async_copy(x_refs[s], rows(s, *me), local_sems.at[s]) for s in sections]
        sent = [copy(k, s, me, to, True) for k, to in enumerate((sibling, (*x_nbr, c), (*y_nbr, c)))
                for s in sections]
        for cp in mine + sent:
            cp.start()
        for s in sections:
            copy(1, s, (*x_nbr, c), me).wait_recv()
            copy(2, s, (*y_nbr, c), me).wait_recv()
            sent += [copy(3, s, (*relay_from, c), (*relay_to, c)),
                     copy(4, s, (*x_nbr, c), sibling), copy(5, s, (*y_nbr, c), sibling)]
            for cp in sent[-3:]:
                cp.start()
        for s in sections:
            copy(3, s, (*diagonal, c), me).wait_recv()
            sent.append(copy(6, s, (*diagonal, c), sibling))
            sent[-1].start()
        for s in sections:
            copy(0, s, sibling, me).wait_recv()
            for k, chip in ((4, x_nbr), (5, y_nbr), (6, diagonal)):
                copy(k, s, (*chip, 1 - c), me).wait_recv()
        for cp in sent:
            cp.wait_send()
        for cp in mine:
            cp.wait()

    return pl.pallas_call(
        body, name=name,
        out_shape=tuple(jax.ShapeDtypeStruct((N_DEV * s.shape[0], s.shape[1]), s.dtype) for s in shards)
        + (jax.ShapeDtypeStruct((1, 1), F32),),
        in_specs=[ANY] * ns,
        out_specs=(ANY,) * ns + (pl.BlockSpec(memory_space=pltpu.SMEM),),
        scratch_shapes=[pltpu.SemaphoreType.DMA((7 * ns,)), pltpu.SemaphoreType.DMA((7 * ns,)),
                        pltpu.SemaphoreType.DMA((ns,))],
    )(*shards)


HBM = pl.BlockSpec(memory_space=pltpu.HBM)
SEM = pl.BlockSpec(memory_space=pltpu.SEMAPHORE)
EFFECT = pltpu.SideEffectType.DATAFLOW_SIDE_EFFECTING


def _in_hbm(a):
    return pltpu.with_memory_space_constraint(a, pltpu.HBM)


def _first_hop_copies(shards, x_refs, zones, send_sems, recv_sems):
    ns = len(shards)
    x, y, c = _position()
    targets = [(x, y, 1 - c)] + [(cx, cy, c) for cx, cy in _other_chips(x, y)]
    return [pltpu.make_async_remote_copy(
        src_ref=x_refs[s], dst_ref=_block_of(zones[s], 4 * x + 2 * y + c, shards[s].shape[0]),
        send_sem=send_sems.at[k * ns + s], recv_sem=recv_sems.at[k * ns + s],
        device_id=to, device_id_type=MESH_ID)
        for k, to in enumerate(targets) for s in range(ns)]


def _after_all(arrays, name):
    def body(*refs):
        refs[-1][...] = jnp.zeros_like(refs[-1])

    return pl.pallas_call(
        body, name=name,
        out_shape=jax.ShapeDtypeStruct((8, LANES), F32),
        in_specs=[pl.BlockSpec(memory_space=pl.ANY)] * len(arrays),
        out_specs=pl.BlockSpec(memory_space=pltpu.VMEM),
    )(*arrays)


def _own_blocks_placed(shards, after):
    ns = len(shards)
    x, y, c = _position()
    me = jnp.reshape(4 * x + 2 * y + c, (1,)).astype(jnp.int32)
    shards = [*shards[:-1], shards[-1] + after.astype(shards[-1].dtype)]

    def body(me_ref, *refs):
        for s in range(ns):
            refs[ns + s][...] = refs[s][...]

    return pl.pallas_call(
        body, name="own_blocks_placed",
        out_shape=tuple(jax.ShapeDtypeStruct((N_DEV * s.shape[0], s.shape[1]), s.dtype) for s in shards),
        grid_spec=pltpu.PrefetchScalarGridSpec(
            num_scalar_prefetch=1, grid=(1,),
            in_specs=[pl.BlockSpec(s.shape, lambda i, me: (0, 0)) for s in shards],
            out_specs=tuple(pl.BlockSpec(s.shape, lambda i, me: (me[0], 0)) for s in shards)),
        compiler_params=_params(dimension_semantics=("arbitrary",)),
    )(me, *shards)


def _gather_start(shards, after, name):
    ns = len(shards)
    zones = _own_blocks_placed(shards, after)

    def body(*refs):
        for cp in _first_hop_copies(shards, refs[:ns], refs[ns:2 * ns], refs[2 * ns], refs[2 * ns + 1]):
            cp.start()
        refs[-1][...] = jnp.zeros_like(refs[-1])

    out = pl.pallas_call(
        body, name=name,
        out_shape=(pltpu.SemaphoreType.DMA((4 * ns,)), pltpu.SemaphoreType.DMA((4 * ns,)),
                   *[pltpu.HBM(a.shape, a.dtype) for a in (*shards, *zones)],
                   jax.ShapeDtypeStruct((8, LANES), F32)),
        in_specs=[HBM] * (2 * ns),
        out_specs=(SEM, SEM, *[HBM] * (2 * ns), pl.BlockSpec(memory_space=pltpu.VMEM)),
        input_output_aliases={i: 2 + i for i in range(2 * ns)},
        compiler_params=pltpu.CompilerParams(has_side_effects=EFFECT),
    )(*[_in_hbm(a) for a in shards], *[_in_hbm(a) for a in zones])
    return out[0], out[1], out[2:2 + ns], out[2 + ns:2 + 2 * ns], out[-1]


def _gather_wait(send_sems, recv_sems, shards, zones, which, after, name):
    ns = len(shards)

    def body(*refs):
        copies = _first_hop_copies(shards, refs[:ns], refs[ns:2 * ns], refs[2 * ns], refs[2 * ns + 1])
        for i, cp in enumerate(copies):
            if i % ns in which:
                cp.wait_send()
                cp.wait_recv()

    out = pl.pallas_call(
        body, name=name,
        out_shape=tuple(pltpu.HBM(a.shape, a.dtype) for a in (*shards, *zones)),
        in_specs=[HBM] * (2 * ns) + [SEM, SEM, ANY],
        out_specs=(HBM,) * (2 * ns),
        input_output_aliases={i: i for i in range(2 * ns)},
        compiler_params=pltpu.CompilerParams(has_side_effects=EFFECT),
    )(*shards, *zones, send_sems, recv_sems, after)
    return out[:ns], out[ns:]


def _gather_pass_on(rows, zones, barrier_id, name):
    ns = len(zones)

    def body(*refs):
        in_refs, out_refs = refs[:ns], refs[ns:2 * ns]
        send_sems, recv_sems = refs[2 * ns:]
        x, y, c = _position()
        _pair_handshake(x, y, c)
        copies = [pltpu.make_async_remote_copy(
            src_ref=_block_of(in_refs[s], 4 * cx + 2 * cy + c, rows[s]),
            dst_ref=_block_of(out_refs[s], 4 * cx + 2 * cy + c, rows[s]),
            send_sem=send_sems.at[j * ns + s], recv_sem=recv_sems.at[j * ns + s],
            device_id=(x, y, 1 - c), device_id_type=MESH_ID)
            for j, (cx, cy) in enumerate(_other_chips(x, y)) for s in range(ns)]
        for cp in copies:
            cp.start()
        for cp in copies:
            cp.wait_recv()
        for cp in copies:
            cp.wait_send()

    return pl.pallas_call(
        body, name=name,
        out_shape=tuple(jax.ShapeDtypeStruct(z.shape, z.dtype) for z in zones),
        in_specs=[ANY] * ns, out_specs=(ANY,) * ns,
        input_output_aliases={i: i for i in range(ns)},
        scratch_shapes=[pltpu.SemaphoreType.DMA((3 * ns,)), pltpu.SemaphoreType.DMA((3 * ns,))],
        compiler_params=pltpu.CompilerParams(collective_id=barrier_id),
    )(*zones)


def _pair_copies(sections, g_refs, land, send_sems, recv_sems):
    ns = len(sections)
    x, y, c = _position()
    return [pltpu.make_async_remote_copy(
        src_ref=_block_of(g_refs[s], 2 * k + 1 - c, rows), dst_ref=land[s].at[k],
        send_sem=send_sems.at[k * ns + s], recv_sem=recv_sems.at[k * ns + s],
        device_id=(x, y, 1 - c), device_id_type=MESH_ID)
        for k in range(N_CHIPS) for s, (_, rows, _) in enumerate(sections)]


def _pair_exchange_start(sections, grads, barrier_id, name):
    ns = len(sections)

    def body(*refs):
        _pair_handshake(*_position())
        for cp in _pair_copies(sections, refs[:ns], refs[ns:2 * ns], refs[2 * ns], refs[2 * ns + 1]):
            cp.start()
        refs[-1][...] = jnp.zeros_like(refs[-1])

    zones = [lax.empty((N_CHIPS, rows, cols), BF16) for _, rows, cols in sections]
    n = N_CHIPS * ns
    out = pl.pallas_call(
        body, name=name,
        out_shape=(pltpu.SemaphoreType.DMA((n,)), pltpu.SemaphoreType.DMA((n,)),
                   *[pltpu.HBM(a.shape, a.dtype) for a in (*grads, *zones)],
                   jax.ShapeDtypeStruct((8, LANES), F32)),
        in_specs=[HBM] * (2 * ns),
        out_specs=(SEM, SEM, *[HBM] * (2 * ns), pl.BlockSpec(memory_space=pltpu.VMEM)),
        input_output_aliases={i: 2 + i for i in range(2 * ns)},
        compiler_params=pltpu.CompilerParams(has_side_effects=EFFECT, collective_id=barrier_id),
    )(*[_in_hbm(a) for a in grads], *[_in_hbm(a) for a in zones])
    return out[0], out[1], out[2:2 + ns], out[2 + ns:2 + 2 * ns], out[-1]


def _pair_exchange_wait(sections, send_sems, recv_sems, grads, zones, after, name):
    ns = len(sections)

    def body(*refs):
        for cp in _pair_copies(sections, refs[:ns], refs[ns:2 * ns], refs[2 * ns], refs[2 * ns + 1]):
            cp.wait_send()
            cp.wait_recv()

    out = pl.pallas_call(
        body, name=name,
        out_shape=tuple(pltpu.HBM(a.shape, a.dtype) for a in (*grads, *zones)),
        in_specs=[HBM] * (2 * ns) + [SEM, SEM, ANY],
        out_specs=(HBM,) * (2 * ns),
        input_output_aliases={i: i for i in range(2 * ns)},
        compiler_params=pltpu.CompilerParams(has_side_effects=EFFECT),
    )(*grads, *zones, send_sems, recv_sems, after)
    return out[:ns], out[ns:]


def _pair_add(sections, grads, got, core, name):
    ns = len(sections)

    def body(core_ref, *refs):
        g_refs, got_refs, p_refs = refs[:ns], refs[ns:2 * ns], refs[2 * ns:]
        for s in range(ns):
            p_refs[s][0] = (g_refs[s][...].astype(F32) + got_refs[s][0].astype(F32)).astype(BF16)

    slot = [pl.BlockSpec((1, rows, cols), lambda k, c: (k, 0, 0)) for _, rows, cols in sections]
    return pl.pallas_call(
        body, name=name,
        out_shape=tuple(jax.ShapeDtypeStruct((N_CHIPS, rows, cols), BF16) for _, rows, cols in sections),
        grid_spec=pltpu.PrefetchScalarGridSpec(
            num_scalar_prefetch=1, grid=(N_CHIPS,),
            in_specs=[pl.BlockSpec((rows, cols), lambda k, c: (2 * k + c[0], 0)) for _, rows, cols in sections]
            + slot,
            out_specs=tuple(slot)),
        compiler_params=_params(dimension_semantics=("parallel",)),
    )(core, *grads, *got)


def _chip_copies(sections, p_refs, land, send_sems, recv_sems):
    ns = len(sections)
    x, y, c = _position()
    return [pltpu.make_async_remote_copy(
        src_ref=p_refs[s].at[2 * cx + cy], dst_ref=land[s].at[j],
        send_sem=send_sems.at[j * ns + s], recv_sem=recv_sems.at[j * ns + s],
        device_id=(cx, cy, c), device_id_type=MESH_ID)
        for j, (cx, cy) in enumerate(_other_chips(x, y)) for s in range(ns)]


def _chip_exchange(sections, parts, name):
    ns = len(sections)

    def body(*refs):
        copies = _chip_copies(sections, refs[:ns], refs[ns:2 * ns], *refs[2 * ns:])
        for cp in copies:
            cp.start()
        for cp in copies:
            cp.wait_recv()
        for cp in copies:
            cp.wait_send()

    n = 3 * ns
    return pl.pallas_call(
        body, name=name,
        out_shape=tuple(jax.ShapeDtypeStruct((3, rows, cols), BF16) for _, rows, cols in sections),
        in_specs=[ANY] * ns, out_specs=(ANY,) * ns,
        scratch_shapes=[pltpu.SemaphoreType.DMA((n,)), pltpu.SemaphoreType.DMA((n,))],
    )(*parts)


def _chip_exchange_start(sections, parts, name):
    ns = len(sections)

    def body(*refs):
        p_refs, land = refs[:ns], refs[ns:2 * ns]
        send_sems, recv_sems = refs[2 * ns], refs[2 * ns + 1]
        token = refs[-1]
        for cp in _chip_copies(sections, p_refs, land, send_sems, recv_sems):
            cp.start()
        token[...] = jnp.zeros_like(token)

    zones = [lax.empty((3, rows, cols), BF16) for _, rows, cols in sections]
    out = pl.pallas_call(
        body, name=name,
        out_shape=(pltpu.SemaphoreType.DMA((3 * ns,)), pltpu.SemaphoreType.DMA((3 * ns,)),
                   *[pltpu.HBM(a.shape, a.dtype) for a in parts], *[pltpu.HBM(a.shape, a.dtype) for a in zones],
                   jax.ShapeDtypeStruct((8, LANES), F32)),
        in_specs=[HBM] * (2 * ns),
        out_specs=(SEM, SEM, *[HBM] * (2 * ns), pl.BlockSpec(memory_space=pltpu.VMEM)),
        input_output_aliases={i: 2 + i for i in range(2 * ns)},
        compiler_params=pltpu.CompilerParams(has_side_effects=EFFECT),
    )(*[_in_hbm(a) for a in parts], *[_in_hbm(a) for a in zones])
    return out[0], out[1], out[2:2 + ns], out[2 + ns:2 + 2 * ns], out[-1]


def _chip_exchange_wait(sections, send_sems, recv_sems, parts, zones, after, name):
    ns = len(sections)

    def body(*refs):
        p_refs, land = refs[:ns], refs[ns:2 * ns]
        for cp in _chip_copies(sections, p_refs, land, refs[2 * ns], refs[2 * ns + 1]):
            cp.wait_send()
            cp.wait_recv()

    out = pl.pallas_call(
        body, name=name,
        out_shape=tuple(pltpu.HBM(a.shape, a.dtype) for a in (*parts, *zones)),
        in_specs=[HBM] * (2 * ns) + [SEM, SEM, ANY],
        out_specs=(HBM,) * (2 * ns),
        input_output_aliases={i: i for i in range(2 * ns)},
        compiler_params=pltpu.CompilerParams(has_side_effects=EFFECT),
    )(*parts, *zones, send_sems, recv_sems, after)
    return out[:ns], out[ns:]


def _grad_finish(sections, parts, far, chip, name):
    ns = len(sections)

    def body(chip_ref, *refs):
        p_refs, b_refs, g_refs = refs[:ns], refs[ns:2 * ns], refs[2 * ns:]
        for s in range(ns):
            g = p_refs[s][0].astype(F32)
            for j in range(3):
                g = g + b_refs[s][j].astype(F32)
            g_refs[s][...] = g

    half = [(rows // 2, cols) for _, rows, cols in sections]
    return pl.pallas_call(
        body, name=name,
        out_shape=tuple(jax.ShapeDtypeStruct((rows, cols), F32) for _, rows, cols in sections),
        grid_spec=pltpu.PrefetchScalarGridSpec(
            num_scalar_prefetch=1, grid=(2,),
            in_specs=[pl.BlockSpec((1, r, c), lambda i, chip: (chip[0], i, 0)) for r, c in half]
            + [pl.BlockSpec((3, r, c), lambda i, chip: (0, i, 0)) for r, c in half],
            out_specs=tuple(pl.BlockSpec((r, c), lambda i, chip: (i, 0)) for r, c in half)),
        compiler_params=_params(dimension_semantics=("parallel",)),
    )(chip, *parts, *far)


def _sum_devices(parts, rows, name):
    cols = parts.shape[1]
    tr = rows // 2

    def body(*refs):
        s = refs[0][...].astype(F32)
        for d in range(1, N_DEV):
            s = s + refs[d][...].astype(F32)
        refs[N_DEV][...] = s

    return pl.pallas_call(
        body, name=name,
        out_shape=jax.ShapeDtypeStruct((rows, cols), F32),
        grid=(2,),
        in_specs=[pl.BlockSpec((tr, cols), lambda i, d=d: (2 * d + i, 0)) for d in range(N_DEV)],
        out_specs=pl.BlockSpec((tr, cols), lambda i: (i, 0)),
        compiler_params=_params(dimension_semantics=("parallel",)),
    )(*([parts] * N_DEV))


def _adamw_step(w_ref, g_ref, m_ref, v_ref, d_ref, nm_ref, nv_ref):
    c1 = 1.0 / (1.0 - ADAM_B1 ** ADAM_STEP)
    c2 = 1.0 / (1.0 - ADAM_B2 ** ADAM_STEP)
    gv = g_ref[...]
    nm = ADAM_B1 * m_ref[...] + (1.0 - ADAM_B1) * gv
    nv = ADAM_B2 * v_ref[...] + (1.0 - ADAM_B2) * (gv * gv)
    nm_ref[...] = nm
    nv_ref[...] = nv
    d_ref[...] = (-ADAM_LR) * ((nm * c1) / (jnp.sqrt(nv * c2) + ADAM_EPS) + ADAM_WD * w_ref[...])


def _adamw_small(params, name):
    n = len(params)

    def body(*refs):
        for k in range(n):
            _adamw_step(*refs[4 * k:4 * k + 4], *refs[4 * n + 3 * k:4 * n + 3 * k + 3])

    out = pl.pallas_call(
        body, name=name,
        out_shape=tuple(jax.ShapeDtypeStruct(p[0].shape, F32) for p in params for _ in range(3)),
    )(*[a for p in params for a in p])
    return [out[3 * k:3 * k + 3] for k in range(n)]


def _adamw(w, g, m, v, name):
    rows, cols = w.shape
    tr = rows
    while tr * cols * 4 > (1 << 20) and tr % 16 == 0:
        tr //= 2

    def body(*refs):
        _adamw_step(*refs)

    spec = pl.BlockSpec((tr, cols), lambda i: (i, 0))
    shape = jax.ShapeDtypeStruct((rows, cols), F32)
    return pl.pallas_call(
        body, name=name,
        out_shape=(shape, shape, shape),
        grid=(rows // tr,),
        in_specs=[spec] * 4, out_specs=(spec,) * 3,
        compiler_params=_params(dimension_semantics=("parallel",)),
    )(w, g, m, v)


NAMES = ("ln1_g", "w_in", "b_in", "rpb", "w_att_o", "conv_w", "conv_b", "w_rg_a", "b_rg_a", "w_rg_i",
         "b_rg_i", "lru_lambda", "w_rec_o", "w_out", "ln2_g", "w_ff1", "w_ff2", "lnf_g")
TRANSPOSED = {"w_in": "w_in_t", "w_att_o": "w_att_o_t", "w_ff1": "w_ff1_t"}
ROW_SHARDED = ("w_rec_o", "w_out", "w_ff2")
REPLICATED = (("ln1_g", (1, D)), ("b_in", (1, D_IN)), ("rpb", (N_HEADS * N_RPB_R, N_RPB_C)),
              ("conv_b", (1, D_REC)), ("w_rg_a", (2 * N_REC_BLOCKS * REC_BLOCK, REC_BLOCK)),
              ("w_rg_i", (2 * N_REC_BLOCKS * REC_BLOCK, REC_BLOCK)), ("ln2_g", (1, D)), ("lnf_g", (1, D)))
GATE_BLOCKS = ("w_rg_a", "w_rg_i")
SMALL_ROWS = 112


def _chan_bits(vectors):
    chan = jnp.concatenate(vectors, axis=0)
    bits = lax.bitcast_convert_type(chan, BF16).reshape(-1)
    return jnp.pad(bits, (0, CHAN_BLOCK_ROWS * D - bits.shape[0])).reshape(CHAN_BLOCK_ROWS, D)


def _chan_from_bits(gathered):
    bits = gathered.reshape(N_DEV, CHAN_BLOCK_ROWS * D)[:, :2 * N_CHAN_ROWS * LANES]
    chan = lax.bitcast_convert_type(bits.reshape(N_DEV, N_CHAN_ROWS, LANES, 2), F32)
    return chan.transpose(1, 0, 2).reshape(N_CHAN_ROWS, D)


def kernel(x, ln1_g, w_in, b_in, rpb, w_att_o, conv_w, conv_b, w_rg_a, b_rg_a, w_rg_i, b_rg_i, lru_lambda, w_rec_o, w_out, ln2_g, w_ff1, w_ff2, lnf_g, loss_target, m_ln1_g, m_w_in, m_b_in, m_rpb, m_w_att_o, m_conv_w, m_conv_b, m_w_rg_a, m_b_rg_a, m_w_rg_i, m_b_rg_i, m_lru_lambda, m_w_rec_o, m_w_out, m_ln2_g, m_w_ff1, m_w_ff2, m_lnf_g, v_ln1_g, v_w_in, v_b_in, v_rpb, v_w_att_o, v_conv_w, v_conv_b, v_w_rg_a, v_b_rg_a, v_w_rg_i, v_b_rg_i, v_lru_lambda, v_w_rec_o, v_w_out, v_ln2_g, v_w_ff1, v_w_ff2, v_lnf_g):
    w = dict(zip(NAMES, (ln1_g, w_in, b_in, rpb, w_att_o, conv_w, conv_b, w_rg_a, b_rg_a, w_rg_i,
                         b_rg_i, lru_lambda, w_rec_o, w_out, ln2_g, w_ff1, w_ff2, lnf_g)))
    m = dict(zip(NAMES, (m_ln1_g, m_w_in, m_b_in, m_rpb, m_w_att_o, m_conv_w, m_conv_b, m_w_rg_a,
                         m_b_rg_a, m_w_rg_i, m_b_rg_i, m_lru_lambda, m_w_rec_o, m_w_out, m_ln2_g,
                         m_w_ff1, m_w_ff2, m_lnf_g)))
    v = dict(zip(NAMES, (v_ln1_g, v_w_in, v_b_in, v_rpb, v_w_att_o, v_conv_w, v_conv_b, v_w_rg_a,
                         v_b_rg_a, v_w_rg_i, v_b_rg_i, v_lru_lambda, v_w_rec_o, v_w_out, v_ln2_g,
                         v_w_ff1, v_w_ff2, v_lnf_g)))
    xi, yi, ci = _position()

    shard = {t: w[n][0].T.astype(BF16) for n, t in TRANSPOSED.items()}
    shard.update({n: w[n][0].astype(BF16) for n in ROW_SHARDED})
    shard["chan"] = _chan_bits([w[n][0] for n, _ in CHAN])
    first, later = ("w_in_t", "chan"), ("w_rec_o", "w_out", "w_att_o_t", "w_ff1_t", "w_ff2")
    *gathered, done = _all_gather([shard[n] for n in first], "weight_all_gather")
    p = dict(zip(first, gathered))
    send_sems, recv_sems, sent, zones, token = _gather_start([shard[n] for n in later], done,
                                                             "weight_gather_start")

    travelling = {"shards": sent, "zones": zones}
    stages = (("w_rec_o", "w_out", "w_att_o_t"), ("w_ff1_t", "w_ff2"))

    def late_weights(after, stage):
        which = [later.index(n) for n in stages[stage]]
        travelling["shards"], travelling["zones"] = _gather_wait(
            send_sems, recv_sems, travelling["shards"], travelling["zones"], which, after,
            "weight_gather_wait_%d" % stage)
        return dict(zip(stages[stage], _gather_pass_on(
            [shard[n].shape[0] for n in stages[stage]], [travelling["zones"][i] for i in which],
            PASS_ON_IDS[stage], "weight_gather_pass_on_%d" % stage)))

    chan = _chan_from_bits(p.pop("chan"))
    r0 = 0
    for n, rows in CHAN:
        p[n] = chan[r0:r0 + rows]
        r0 += rows
    p.update(ln1_g=w["ln1_g"], b_in=w["b_in"] + token[0, 0], rpb=w["rpb"][0], conv_b=w["conv_b"],
             w_rg_a=w["w_rg_a"][0], w_rg_i=w["w_rg_i"][0], ln2_g=w["ln2_g"],
             lnf_g=w["lnf_g"].reshape(1, D))

    core = jnp.reshape(ci, (1,)).astype(jnp.int32)
    chip = jnp.reshape(2 * xi + yi, (1,)).astype(jnp.int32)
    first_sections = tuple(s for s in SECTIONS if s[0] in ("w_ff1_t", "w_ff2"))
    late_sections = SECTIONS[:1]
    early_sections = tuple(s for s in SECTIONS[1:] if s not in first_sections)
    in_flight = {}

    def pair_sum_and_send(group, sections, after):
        send_sems, recv_sems, sect, zones, _ = in_flight["pair_" + group]
        sect, got = _pair_exchange_wait(sections, send_sems, recv_sems, sect, zones, after,
                                        "grad_pair_exchange_wait_" + group)
        parts = _pair_add(sections, sect, got, core, "grad_pair_add_" + group)
        in_flight[group] = _chip_exchange_start(sections, parts, "grad_chip_exchange_start_" + group)
        return in_flight[group][-1]

    def pair_exchange_at_once(group, sections, grads, barrier_id):
        in_flight["pair_" + group] = _pair_exchange_start(
            sections, [grads[n] for n, _, _ in sections], barrier_id, "grad_pair_exchange_start_" + group)
        return pair_sum_and_send(group, sections, in_flight["pair_" + group][-1])

    def reduce_first(grads):
        return pair_exchange_at_once("first", first_sections, grads, PAIR_FIRST_ID)

    def reduce_early(grads):
        chan_g = jnp.concatenate([grads[n] for n, _ in CHAN], axis=0)
        chan_g = chan_g.reshape(N_CHAN_ROWS, N_DEV, LANES).transpose(1, 0, 2).astype(BF16)
        chan_g = jnp.pad(chan_g.reshape(N_DEV, -1), ((0, 0), (0, CHAN_BLOCK_ROWS * D - N_CHAN_ROWS * LANES)))
        grads["chan"] = chan_g.reshape(N_DEV * CHAN_BLOCK_ROWS, D)
        grads["gates"] = jnp.concatenate([grads[n].reshape(-1, D) for n in GATE_BLOCKS], axis=0).astype(BF16)
        return pair_exchange_at_once("early", early_sections, grads, PAIR_EARLY_ID)[0, 0]

    loss_part, grad_x, grads = _local_step(x[0], loss_target[0], p, late_weights, reduce_first, reduce_early)
    in_flight["pair_late"] = _pair_exchange_start(
        late_sections, [grads[n] for n, _, _ in late_sections], PAIR_LATE_ID, "grad_pair_exchange_start_late")

    def finish(group, sections, after, name):
        send_sems, recv_sems, parts, zones, _ = in_flight[group]
        parts, far = _chip_exchange_wait(sections, send_sems, recv_sems, parts, zones, after,
                                         "grad_chip_exchange_wait_" + name)
        return dict(zip((n for n, _, _ in sections),
                        _grad_finish(sections, parts, far, chip, "grad_finish_" + name)))

    summed = finish("first", first_sections, in_flight["pair_late"][-1], "first")
    summed.update(finish("early", early_sections, summed["w_ff2"], "early"))
    started_late = pair_sum_and_send("late", late_sections, summed["gates"])

    flat = jnp.concatenate([grads[n].reshape(-1) for n, _ in REPLICATED if n not in GATE_BLOCKS]
                           + [loss_part.reshape(-1) + started_late[0, 0]])
    n_small = flat.shape[0]
    flat = jnp.pad(flat, (0, SMALL_ROWS * LANES - n_small)).reshape(SMALL_ROWS, LANES)
    small_parts, gate_sum, _ = _all_gather([flat, summed["gates"]], "small_grad_all_gather")
    small = _sum_devices(small_parts, SMALL_ROWS, "small_grad_sum").reshape(-1)
    loss = small[n_small - 1]

    g, delta, new_m, new_v = {}, {}, {}, {}

    def update(n, g2, shape2):
        d2, m2, v2 = _adamw(w[n].reshape(shape2), g2, m[n].reshape(shape2), v[n].reshape(shape2),
                            "adamw_" + n)
        g[n], delta[n], new_m[n], new_v[n] = (a.reshape(w[n].shape) for a in (g2, d2, m2, v2))

    small_params = []
    o = 0
    for n, shape2 in REPLICATED:
        if n in GATE_BLOCKS:
            k, rows = GATE_BLOCKS.index(n), gate_sum.shape[0] // len(GATE_BLOCKS)
            update(n, gate_sum[k * rows:(k + 1) * rows].reshape(shape2), shape2)
        else:
            size = shape2[0] * shape2[1]
            small_params.append((n, small[o:o + size].reshape(shape2), shape2))
            o += size
    chan_back = summed["chan"].reshape(-1)[:N_CHAN_ROWS * LANES].reshape(N_CHAN_ROWS, LANES)
    r0 = 0
    for n, rows in CHAN:
        small_params.append((n, chan_back[r0:r0 + rows], (rows, LANES)))
        r0 += rows
    results = _adamw_small([(w[n].reshape(s2), g2, m[n].reshape(s2), v[n].reshape(s2))
                            for n, g2, s2 in small_params], "adamw_vectors")
    for (n, g2, _), (d2, m2, v2) in zip(small_params, results):
        g[n], delta[n], new_m[n], new_v[n] = (a.reshape(w[n].shape) for a in (g2, d2, m2, v2))

    for n in ROW_SHARDED:
        update(n, summed[n], summed[n].shape)
    for n, t in TRANSPOSED.items():
        if t in summed:
            update(n, summed[t].T, summed[t].shape[::-1])
    summed = finish("late", late_sections, _after_all(list(delta.values()), "updates_done"), "late")
    update("w_in", summed["w_in_t"].T, summed["w_in_t"].shape[::-1])

    return (loss, grad_x[None], *[g[n] for n in NAMES], *[delta[n] for n in NAMES],
            *[new_m[n] for n in NAMES], *[new_v[n] for n in NAMES])
```

```python
import math

import numpy as np
import jax
import jax.numpy as jnp
from jax import lax
from jax.experimental import pallas as pl
from jax.experimental.pallas import tpu as pltpu

F32 = jnp.float32
BF16 = jnp.bfloat16

T = 2048
D = 1024
D_ATT = 512
D_REC = 1024
D_FF = 4096
D_IN = 5632
N_HEADS = 8
DH = 64
GRID_W = 64
ROWS = T // GRID_W
WIN_H = 8
WIN_W = 16
KWIN = WIN_H * GRID_W
N_RPB_R = 2 * WIN_H - 1
N_RPB_C = 2 * WIN_W - 1
N_REC_BLOCKS = 16
REC_BLOCK = 64
CG = 128
N_CG = D_REC // CG
LRU_C = 8.0
EPS = 1e-6
N_DEV = 8
N_CHIPS = 4
LANES = 128

ADAM_LR = 0.001
ADAM_B1 = 0.9
ADAM_B2 = 0.999
ADAM_EPS = 1e-08
ADAM_WD = 0.01
ADAM_STEP = 10

MESH_AXES = ("x", "y", "c")
VMEM_LIMIT = 56 * 1024 * 1024

TILE = 512
DZ_ARRAYS = ((0, 3, 1), (3, 4, 2), (7, 4, 2))
N_DZ_TILES = D_IN // TILE


def _params(**kw):
    return pltpu.CompilerParams(vmem_limit_bytes=VMEM_LIMIT, **kw)


HG = 4
HQ = HG * GRID_W
HC = HG * DH


def _att_tables():
    rq = np.arange(GRID_W)
    kc = np.arange(KWIN) % GRID_W
    win_start = np.clip(rq - WIN_W // 2, 0, GRID_W - WIN_W)
    valid = (kc[None, :] >= win_start[:, None]) & (kc[None, :] < win_start[:, None] + WIN_W)
    same_head = (np.arange(HQ)[:, None] // GRID_W) == (np.arange(HC)[None, :] // DH)
    return valid.astype(np.float32), same_head.astype(np.float32)


def _pair_mask():
    half = np.arange(2 * DH) // DH
    return (half[:, None] == half[None, :]).astype(np.float32)


def _dup_table():
    return np.concatenate([np.eye(REC_BLOCK, dtype=np.float32)] * 2, axis=1)


def _sigmoid(x):
    return 0.5 * jnp.tanh(0.5 * x) + 0.5


def _softplus(x):
    return jnp.maximum(x, 0.0) + jnp.log(1.0 + jnp.exp(-jnp.abs(x)))


def _one_minus_square(log_a, a):
    x = 2.0 * log_a
    series = -x * (1.0 + x * (0.5 + x * (1.0 / 6.0)))
    return jnp.where(x > -0.02, series, 1.0 - a * a)


_GELU_C = math.sqrt(2.0 / math.pi)


def _gelu_and_grad(x):
    x2 = x * x
    inner = _GELU_C * (x + 0.044715 * x * x2)
    t = jnp.tanh(inner)
    g = 0.5 * x * (1.0 + t)
    dg = 0.5 * (1.0 + t) + 0.5 * x * (1.0 - t * t) * _GELU_C * (1.0 + 3.0 * 0.044715 * x2)
    return g, dg


def _dot(a, b):
    return jnp.dot(a, b, preferred_element_type=F32)


def _dot_nt(a, b):
    return lax.dot_general(a, b, (((1,), (1,)), ((), ())), preferred_element_type=F32)


def _dot_tn(a, b):
    return lax.dot_general(a, b, (((0,), (0,)), ((), ())), preferred_element_type=F32)


def _dot_exact(a, b):
    return jnp.dot(a, b, precision=lax.Precision.HIGHEST, preferred_element_type=F32)


def _shift_rows(x, s):
    n = x.shape[0]
    rows = lax.broadcasted_iota(jnp.int32, x.shape, 0)
    y = pltpu.roll(x, s % n, 0)
    if s > 0:
        return jnp.where(rows >= s, y, 0.0)
    return jnp.where(rows < n + s, y, 0.0)


def _rms_bwd(dh, xh, r, g):
    dxh = dh * g
    return r * (dxh - xh * jnp.mean(dxh * xh, axis=-1, keepdims=True))


def _matmul(a, b, mode, out_dtype, name, tm=512, tn=1024, tk=2048):
    if mode == "nn":
        (m, k), (k2, n) = a.shape, b.shape
    elif mode == "nt":
        (m, k), (n, k2) = a.shape, b.shape
    else:
        (k, m), (k2, n) = a.shape, b.shape
    assert k == k2
    tm, tn, tk = min(tm, m), min(tn, n), min(tk, k)
    assert m % tm == 0 and n % tn == 0 and k % tk == 0
    nk = k // tk
    dot = {"nn": _dot, "nt": _dot_nt, "tn": _dot_tn}[mode]

    def body(a_ref, b_ref, o_ref, acc):
        kk = pl.program_id(2)
        part = dot(a_ref[...].astype(BF16), b_ref[...].astype(BF16))
        if nk == 1:
            o_ref[...] = part.astype(out_dtype)
            return

        @pl.when(kk == 0)
        def _():
            acc[...] = part

        @pl.when(kk > 0)
        def _():
            acc[...] += part

        @pl.when(kk == nk - 1)
        def _():
            o_ref[...] = acc[...].astype(out_dtype)

    if mode == "tn":
        a_spec = pl.BlockSpec((tk, tm), lambda i, j, kk: (kk, i))
    else:
        a_spec = pl.BlockSpec((tm, tk), lambda i, j, kk: (i, kk))
    if mode == "nt":
        b_spec = pl.BlockSpec((tn, tk), lambda i, j, kk: (j, kk))
    else:
        b_spec = pl.BlockSpec((tk, tn), lambda i, j, kk: (kk, j))
    return pl.pallas_call(
        body, name=name,
        out_shape=jax.ShapeDtypeStruct((m, n), out_dtype),
        grid=(m // tm, n // tn, nk),
        in_specs=[a_spec, b_spec],
        out_specs=pl.BlockSpec((tm, tn), lambda i, j, kk: (i, j)),
        scratch_shapes=[pltpu.VMEM((tm, tn) if nk > 1 else (8, LANES), F32)],
        compiler_params=_params(dimension_semantics=("parallel", "parallel", "arbitrary")),
    )(a, b)


def _in_proj(x, g1, w_in_t, b_in):
    tm = 512

    def body(x_ref, g_ref, w_hbm, b_ref, qkv_ref, uy_ref, gg_ref, h_ref, w):
        @pl.when(pl.program_id(0) == 0)
        def _():
            pltpu.sync_copy(w_hbm, w)

        xv = x_ref[...]
        r = lax.rsqrt(jnp.mean(xv * xv, axis=-1, keepdims=True) + EPS)
        h = ((xv * r) * g_ref[...]).astype(BF16)
        h_ref[...] = h
        row0 = 0
        for ref in (qkv_ref, uy_ref, gg_ref):
            for c0 in range(0, ref.shape[1], TILE):
                z = _dot_nt(h, w[row0:row0 + TILE, :]) + b_ref[:, row0:row0 + TILE]
                ref[:, c0:c0 + TILE] = z.astype(ref.dtype)
                row0 += TILE

    tok = lambda width: pl.BlockSpec((tm, width), lambda i: (i, 0))
    return pl.pallas_call(
        body, name="in_proj",
        out_shape=(jax.ShapeDtypeStruct((T, 3 * D_ATT), BF16),
                   jax.ShapeDtypeStruct((T, 2 * D_REC), F32),
                   jax.ShapeDtypeStruct((T, 2 * D), F32),
                   jax.ShapeDtypeStruct((T, D), BF16)),
        grid=(T // tm,),
        in_specs=[tok(D), pl.BlockSpec((1, D), lambda i: (0, 0)), pl.BlockSpec(memory_space=pl.ANY),
                  pl.BlockSpec((1, D_IN), lambda i: (0, 0))],
        out_specs=(tok(3 * D_ATT), tok(2 * D_REC), tok(2 * D), tok(D)),
        scratch_shapes=[pltpu.VMEM((D_IN, D), BF16)],
        compiler_params=_params(dimension_semantics=("arbitrary",)),
    )(x, g1, w_in_t, b_in)


def _dz_specs(rows, tile_of, row_of):
    def spec(off, n, per_plane):
        def index(*ids):
            t = jnp.clip(tile_of(*ids) - off, 0, n - 1)
            return (t // per_plane, row_of(*ids), t % per_plane)
        return pl.BlockSpec((1, rows, TILE), index)
    return [spec(off, n, per) for off, n, per in DZ_ARRAYS]


def _dh_norm1_bwd(dz, w_in_t, x, g1, dx1):
    tm = 512

    def body(dqkv_ref, duy_ref, dgg_ref, w_hbm, x_ref, g_ref, dx1_ref, gx_ref, dg_ref, w):
        @pl.when(pl.program_id(0) == 0)
        def _():
            pltpu.sync_copy(w_hbm, w)
            dg_ref[...] = jnp.zeros_like(dg_ref)

        dh, row0 = None, 0
        for ref in (dqkv_ref, duy_ref, dgg_ref):
            for plane in range(ref.shape[0]):
                cols = ref.shape[2]
                part = _dot(ref[plane], w[row0:row0 + cols, :])
                dh = part if dh is None else dh + part
                row0 += cols
        xv = x_ref[...]
        r = lax.rsqrt(jnp.mean(xv * xv, axis=-1, keepdims=True) + EPS)
        xh = xv * r
        dg_ref[...] += jnp.sum(dh * xh, axis=0, keepdims=True)
        gx_ref[...] = dx1_ref[...] + _rms_bwd(dh, xh, r, g_ref[...])

    tok = pl.BlockSpec((tm, D), lambda i: (i, 0))
    vec = pl.BlockSpec((1, D), lambda i: (0, 0))
    planes = lambda a: pl.BlockSpec((a.shape[0], tm, a.shape[2]), lambda i: (0, i, 0))
    return pl.pallas_call(
        body, name="dh_norm1_bwd",
        out_shape=(jax.ShapeDtypeStruct((T, D), F32), jax.ShapeDtypeStruct((1, D), F32)),
        grid=(T // tm,),
        in_specs=[planes(a) for a in dz] + [pl.BlockSpec(memory_space=pl.ANY), tok, vec, tok],
        out_specs=(tok, vec),
        scratch_shapes=[pltpu.VMEM((D_IN, D), BF16)],
        compiler_params=_params(dimension_semantics=("arbitrary",)),
    )(*dz, w_in_t, x, g1, dx1)


def _grad_w_in(dz, h):
    def body(*refs):
        seg_refs = refs[:3]
        h_ref, gw_ref, gb_ref = refs[3:]
        j = pl.program_id(0)

        for s, (off, n, _) in enumerate(DZ_ARRAYS):
            @pl.when((j >= off) & (j < off + n))
            def _(s=s):
                a = seg_refs[s][0]
                gw_ref[...] = _dot_tn(a, h_ref[...]).astype(BF16)
                gb_ref[...] = jnp.sum(a.astype(F32), axis=0, keepdims=True)

    return pl.pallas_call(
        body, name="grad_w_in",
        out_shape=(jax.ShapeDtypeStruct((D_IN, D), BF16), jax.ShapeDtypeStruct((1, D_IN), F32)),
        grid=(N_DZ_TILES,),
        in_specs=_dz_specs(T, lambda j: j, lambda j: 0) + [pl.BlockSpec((T, D), lambda j: (0, 0))],
        out_specs=(pl.BlockSpec((TILE, D), lambda j: (j, 0)), pl.BlockSpec((1, TILE), lambda j: (0, j))),
        compiler_params=_params(dimension_semantics=("parallel",)),
    )(*dz, h)


def _rpb_rows(rpb):
    padded = jnp.pad(rpb, ((0, 0), (0, 0), (0, GRID_W - N_RPB_C)))
    rows = [padded[:, WIN_H - 1 - oi: 2 * WIN_H - 1 - oi].reshape(N_HEADS // HG, HG, KWIN)
            for oi in range(WIN_H)]
    return jnp.stack(rows, axis=0)


SKEW = KWIN - (WIN_W - 1)


MASKED = -1e30


def _bias_tiles(rows_ref, valid, bias_s):
    for oi in range(WIN_H):
        for hh in range(HG):
            row = jnp.broadcast_to(rows_ref[oi, 0, hh:hh + 1, :], (GRID_W, KWIN))
            tile = pltpu.roll(row, SKEW, 1, stride=1, stride_axis=0)
            bias_s[oi, hh * GRID_W:(hh + 1) * GRID_W, :] = jnp.where(valid, tile, MASKED)


def _bias_tile_grads(gb_s, flip, out_ref):
    for oi in range(WIN_H):
        for hh in range(HG):
            g = _dot_exact(flip, gb_s[oi, hh * GRID_W:(hh + 1) * GRID_W, :])
            back = pltpu.roll(g, KWIN - (GRID_W - WIN_W), 1, stride=1, stride_axis=0)
            out_ref[0, oi, hh:hh + 1, :] = jnp.sum(back, axis=0, keepdims=True)


def _rpb_fold(row_grads):
    g = row_grads.transpose(1, 0, 2, 3).reshape(WIN_H, N_HEADS, WIN_H, GRID_W)
    g = g.transpose(0, 2, 1, 3)

    def body(g_ref, o_ref):
        for dr in range(N_RPB_R):
            terms = [g_ref[oi, i] for oi in range(WIN_H) for i in range(WIN_H) if i - oi + WIN_H - 1 == dr]
            acc = terms[0]
            for term in terms[1:]:
                acc = acc + term
            o_ref[dr] = acc

    out = pl.pallas_call(
        body, name="rpb_fold",
        out_shape=jax.ShapeDtypeStruct((N_RPB_R, N_HEADS, GRID_W), F32),
    )(g)
    return out.transpose(1, 0, 2)[:, :, :N_RPB_C]


ATT_GROUPS = N_HEADS // HG
ATT_UNROLL = 8


def _stacked(rows64, same_head):
    return jnp.where(same_head, jnp.concatenate([rows64] * HG, axis=0), jnp.zeros((), BF16))


def _own_heads(stacked):
    head = lax.broadcasted_iota(jnp.int32, (GRID_W, HC), 1) // DH
    out = stacked[:GRID_W]
    for h in range(1, HG):
        out = jnp.where(head == h, stacked[h * GRID_W:(h + 1) * GRID_W], out)
    return out


def _att_scores(q_ref, k_ref, bias_ref, same_head, r):
    rs = jnp.clip(r - WIN_H // 2, 0, ROWS - WIN_H)
    oi = r - rs
    q0 = pl.multiple_of(r * GRID_W, GRID_W)
    k0 = pl.multiple_of(rs * GRID_W, GRID_W)
    q2 = _stacked(q_ref[pl.ds(q0, GRID_W), :] * (DH ** -0.5), same_head)
    kw = k_ref[pl.ds(k0, KWIN), :]
    s = _dot_nt(q2, kw) + bias_ref[oi]
    e = jnp.exp(s - jnp.max(s, axis=-1, keepdims=True))
    return e, 1.0 / jnp.sum(e, axis=-1, keepdims=True), q2, kw, q0, k0, oi


def _att_specs():
    col = lambda off: pl.BlockSpec((T, HC), lambda g: (0, g + off * ATT_GROUPS))
    tables = [pl.BlockSpec((WIN_H, 1, HG, KWIN), lambda g: (0, g, 0, 0)),
              pl.BlockSpec((GRID_W, KWIN), lambda g: (0, 0)),
              pl.BlockSpec((HQ, HC), lambda g: (0, 0))]
    return col, tables, pltpu.VMEM((WIN_H, HQ, KWIN), F32)


def _att_fwd(qkv, bias_rows):
    valid_np, same_head_np = _att_tables()

    def body(q_ref, k_ref, v_ref, rows_ref, valid_ref, head_ref, o_ref, bias_s):
        same_head = head_ref[...] > 0.5
        _bias_tiles(rows_ref, valid_ref[...] > 0.5, bias_s)

        def row(r, carry):
            e, rl, _, _, q0, k0, _ = _att_scores(q_ref, k_ref, bias_s, same_head, r)
            o2 = _dot((e * rl).astype(BF16), v_ref[pl.ds(k0, KWIN), :])
            o_ref[pl.ds(q0, GRID_W), :] = _own_heads(o2).astype(BF16)
            return carry

        lax.fori_loop(0, ROWS, row, 0, unroll=ATT_UNROLL)

    col, tables, tiles = _att_specs()
    return pl.pallas_call(
        body, name="att_fwd",
        out_shape=jax.ShapeDtypeStruct((T, D_ATT), BF16),
        grid=(ATT_GROUPS,),
        in_specs=[col(0), col(1), col(2)] + tables,
        out_specs=col(0),
        scratch_shapes=[tiles],
        compiler_params=_params(dimension_semantics=("parallel",)),
    )(qkv, qkv, qkv, bias_rows, jnp.asarray(valid_np), jnp.asarray(same_head_np))


def _att_bwd(qkv, bias_rows, datt, after):
    valid_np, same_head_np = _att_tables()

    def body(q_ref, k_ref, v_ref, do_ref, rows_ref, valid_ref, head_ref, flip_ref,
             dqkv_ref, grows_ref, dk_acc, dv_acc, bias_s, gb_s):
        same_head = head_ref[...] > 0.5
        dk_acc[...] = jnp.zeros_like(dk_acc)
        dv_acc[...] = jnp.zeros_like(dv_acc)
        gb_s[...] = jnp.zeros_like(gb_s)
        _bias_tiles(rows_ref, valid_ref[...] > 0.5, bias_s)

        def row(r, carry):
            e, rl, q2, kw, q0, k0, oi = _att_scores(q_ref, k_ref, bias_s, same_head, r)
            do2 = _stacked(do_ref[pl.ds(q0, GRID_W), :], same_head)
            vw = v_ref[pl.ds(k0, KWIN), :]
            p = e * rl
            dp = _dot_nt(do2, vw)
            ds = p * (dp - jnp.sum(dp * p, axis=-1, keepdims=True))
            p16 = p.astype(BF16)
            ds16 = ds.astype(BF16)
            dv_acc[pl.ds(k0, KWIN), :] += _dot_tn(p16, do2)
            dk_acc[pl.ds(k0, KWIN), :] += _dot_tn(ds16, q2)
            dq2 = _dot(ds16, kw) * (DH ** -0.5)
            dqkv_ref[0, pl.ds(q0, GRID_W), :] = _own_heads(dq2).astype(BF16)
            gb_s[oi] += ds
            return carry

        lax.fori_loop(0, ROWS, row, 0, unroll=ATT_UNROLL)
        dqkv_ref[1] = dk_acc[...].astype(BF16)
        dqkv_ref[2] = dv_acc[...].astype(BF16)
        _bias_tile_grads(gb_s, flip_ref[...], grows_ref)

    col, tables, tiles = _att_specs()
    return pl.pallas_call(
        body, name="att_bwd",
        out_shape=(jax.ShapeDtypeStruct((3, T, D_ATT), BF16),
                   jax.ShapeDtypeStruct((ATT_GROUPS, WIN_H, HG, KWIN), F32)),
        grid=(ATT_GROUPS,),
        in_specs=[col(0), col(1), col(2), col(0)] + tables + [pl.BlockSpec((GRID_W, GRID_W), lambda g: (0, 0))],
        out_specs=(pl.BlockSpec((3, T, HC), lambda g: (0, 0, g)),
                   pl.BlockSpec((1, WIN_H, HG, KWIN), lambda g: (g, 0, 0, 0))),
        scratch_shapes=[pltpu.VMEM((T, HC), F32), pltpu.VMEM((T, HC), F32), tiles, tiles],
        compiler_params=_params(dimension_semantics=("parallel",)),
    )(qkv, qkv, qkv, datt, bias_rows, jnp.asarray(valid_np) + after, jnp.asarray(same_head_np),
      jnp.asarray(np.eye(GRID_W, dtype=np.float32)[::-1].copy()))


def _conv_taps(up):
    return (_shift_rows(up, 2), _shift_rows(up, 1), up, _shift_rows(up, -1))


def _pair_block_diag(w_pair, dup, same_half):
    return jnp.where(same_half, _dot(w_pair.astype(BF16), dup), 0.0).astype(BF16)


def _gates(u, u16, wa, ba, wi, bi, lam):
    r = _sigmoid(_dot(u16, wa) + ba)
    ig = _sigmoid(_dot(u16, wi) + bi)
    sp = _softplus(-lam)
    log_a = (-LRU_C) * r * sp
    a = jnp.exp(log_a)
    mult2 = jnp.maximum(_one_minus_square(log_a, a), 0.0)
    return r, ig, sp, a, jnp.sqrt(mult2), mult2


SCAN_BLOCKS = 8


def _scans(jobs):
    c = jobs[0][0].shape[1]
    nblk = T // 8
    rows = lax.broadcasted_iota(jnp.int32, (8, c), 0)

    def block(a, b, reverse):
        for s in (1, 2, 4):
            if reverse:
                keep = rows < 8 - s
                a_s = jnp.where(keep, pltpu.roll(a, 8 - s, 0), 1.0)
                b_s = jnp.where(keep, pltpu.roll(b, 8 - s, 0), 0.0)
            else:
                keep = rows >= s
                a_s = jnp.where(keep, pltpu.roll(a, s, 0), 1.0)
                b_s = jnp.where(keep, pltpu.roll(b, s, 0), 0.0)
            b = a * b_s + b
            a = a * a_s
        return a, b

    def step(i, carry):
        out = []
        for (a_ref, b_ref, h_ref, reverse), h_prev in zip(jobs, carry):
            for u in range(SCAN_BLOCKS):
                blk = i * SCAN_BLOCKS + u
                if reverse:
                    blk = nblk - 1 - blk
                t0 = pl.multiple_of(blk * 8, 8)
                a, b = block(a_ref[pl.ds(t0, 8), :], b_ref[pl.ds(t0, 8), :], reverse)
                h = a * h_prev + b
                h_ref[pl.ds(t0, 8), :] = h
                h_prev = jnp.broadcast_to(h[0:1] if reverse else h[7:8], (8, c))
            out.append(h_prev)
        return tuple(out)

    lax.fori_loop(0, nblk // SCAN_BLOCKS, step, tuple(jnp.zeros((8, c), F32) for _ in jobs))


def _rec_specs():
    tok = lambda off: pl.BlockSpec((T, CG), lambda g: (0, g + off))
    per_ch = lambda rows: pl.BlockSpec((rows, CG), lambda g: (0, g))
    wspec = pl.BlockSpec((2, 1, CG, REC_BLOCK), lambda g: (0, g, 0, 0))
    const = lambda shape: pl.BlockSpec(shape, lambda g: (0, 0))
    return tok, per_ch, wspec, const


def _rec_fwd(uy, conv_w, conv_b, w_a, b_a, w_i, b_i, lam):
    tok, per_ch, wspec, const = _rec_specs()

    def body(up_ref, yb_ref, cw_ref, cb_ref, wa_ref, ba_ref, wi_ref, bi_ref, lam_ref, dup_ref, half_ref,
             hf_ref, hb_ref, yrec_ref, am_ref, bx_f, bx_b):
        dup = dup_ref[...]
        same_half = half_ref[...] > 0.5
        taps = _conv_taps(up_ref[...])
        u = cb_ref[...]
        for j in range(4):
            u = u + taps[j] * cw_ref[j:j + 1, :]
        u16 = u.astype(BF16)
        for d, bx_s in enumerate((bx_f, bx_b)):
            wa = _pair_block_diag(wa_ref[d, 0], dup, same_half)
            wi = _pair_block_diag(wi_ref[d, 0], dup, same_half)
            _, ig, _, a, mult, _ = _gates(u, u16, wa, ba_ref[d:d + 1, :], wi, bi_ref[d:d + 1, :],
                                       lam_ref[d:d + 1, :])
            am_ref[2 * d] = a
            am_ref[2 * d + 1] = mult
            bx_s[...] = mult * (ig * u)
        _scans([(am_ref.at[0], bx_f, hf_ref, False), (am_ref.at[2], bx_b, hb_ref, True)])
        gelu, _ = _gelu_and_grad(yb_ref[...])
        yrec_ref[...] = ((hf_ref[...] + hb_ref[...]) * gelu).astype(BF16)

    return pl.pallas_call(
        body, name="rec_fwd",
        out_shape=(jax.ShapeDtypeStruct((T, D_REC), F32), jax.ShapeDtypeStruct((T, D_REC), F32),
                   jax.ShapeDtypeStruct((T, D_REC), BF16), jax.ShapeDtypeStruct((4, T, D_REC), F32)),
        grid=(N_CG,),
        in_specs=[tok(0), tok(N_CG), per_ch(4), per_ch(1), wspec, per_ch(2), wspec, per_ch(2), per_ch(2),
                  const((REC_BLOCK, CG)), const((CG, CG))],
        out_specs=(tok(0), tok(0), tok(0), pl.BlockSpec((4, T, CG), lambda g: (0, 0, g))),
        scratch_shapes=[pltpu.VMEM((T, CG), F32)] * 2,
        compiler_params=_params(dimension_semantics=("parallel",)),
    )(uy, uy, conv_w, conv_b, w_a, b_a, w_i, b_i, lam,
      jnp.asarray(_dup_table(), BF16), jnp.asarray(_pair_mask()))


def _rec_bwd(uy, hf, hb, am, dyrec, conv_w, conv_b, w_a, b_a, w_i, b_i, lam):
    tok, per_ch, wspec, const = _rec_specs()

    def body(up_ref, yb_ref, hf_ref, hb_ref, am_ref, dy_ref, cw_ref, cb_ref, wa_ref, ba_ref, wi_ref, bi_ref,
             lam_ref, dup_ref, dupt_ref, half_ref,
             duy_ref, dcw_ref, dcb_ref, dwa_ref, dba_ref, dwi_ref, dbi_ref, dlam_ref,
             a_s0, a_s1, dh_s, g_s0, g_s1):
        dup = dup_ref[...]
        dup_t = dupt_ref[...]
        same_half = half_ref[...] > 0.5
        taps = _conv_taps(up_ref[...])
        u = cb_ref[...]
        for j in range(4):
            u = u + taps[j] * cw_ref[j:j + 1, :]
        u16 = u.astype(BF16)
        gelu, dgelu = _gelu_and_grad(yb_ref[...])
        dy = dy_ref[...]
        duy_ref[1] = (dy * (hf_ref[...] + hb_ref[...]) * dgelu).astype(BF16)
        dh_s[...] = dy * gelu
        a_s0[...] = _shift_rows(am_ref[0], -1)
        a_s1[...] = _shift_rows(am_ref[2], 1)
        _scans([(a_s0, dh_s, g_s0, True), (a_s1, dh_s, g_s1, False)])
        du = jnp.zeros((T, CG), F32)
        for d, g_s in enumerate((g_s0, g_s1)):
            reverse = d == 1
            wa = _pair_block_diag(wa_ref[d, 0], dup, same_half)
            wi = _pair_block_diag(wi_ref[d, 0], dup, same_half)
            lam_d = lam_ref[d:d + 1, :]
            r = _sigmoid(_dot(u16, wa) + ba_ref[d:d + 1, :])
            ig = _sigmoid(_dot(u16, wi) + bi_ref[d:d + 1, :])
            sp = _softplus(-lam_d)
            a, mult = am_ref[2 * d], am_ref[2 * d + 1]
            mult2 = mult * mult
            g = g_s[...]
            h_prev = _shift_rows(hb_ref[...], -1) if reverse else _shift_rows(hf_ref[...], 1)
            da = g * h_prev
            dmult = g * (ig * u)
            dig = g * mult * u
            du = du + g * mult * ig
            dmult_dlog = jnp.where(mult2 > 0.0, -(a * a) * lax.rsqrt(mult2), 0.0)
            dlog_a = da * a + dmult * dmult_dlog
            dr = dlog_a * ((-LRU_C) * sp)
            dsp = jnp.sum(dlog_a * ((-LRU_C) * r), axis=0, keepdims=True)
            dlam_ref[d:d + 1, :] = dsp * (-_sigmoid(-lam_d))
            dga = dr * r * (1.0 - r)
            dgi = dig * ig * (1.0 - ig)
            dga16 = dga.astype(BF16)
            dgi16 = dgi.astype(BF16)
            du = du + _dot_nt(dga16, wa) + _dot_nt(dgi16, wi)
            dwa_ref[d, 0] = _dot_exact(jnp.where(same_half, _dot_tn(u16, dga16), 0.0), dup_t)
            dwi_ref[d, 0] = _dot_exact(jnp.where(same_half, _dot_tn(u16, dgi16), 0.0), dup_t)
            dba_ref[d:d + 1, :] = jnp.sum(dga, axis=0, keepdims=True)
            dbi_ref[d:d + 1, :] = jnp.sum(dgi, axis=0, keepdims=True)
        dcb_ref[...] = jnp.sum(du, axis=0, keepdims=True)
        for j in range(4):
            dcw_ref[j:j + 1, :] = jnp.sum(du * taps[j], axis=0, keepdims=True)
        dup_in = (_shift_rows(du, -2) * cw_ref[0:1, :] + _shift_rows(du, -1) * cw_ref[1:2, :]
                  + du * cw_ref[2:3, :] + _shift_rows(du, 1) * cw_ref[3:4, :])
        duy_ref[0] = dup_in.astype(BF16)

    wshape = jax.ShapeDtypeStruct((2, N_CG, CG, REC_BLOCK), F32)
    vec = lambda rows: jax.ShapeDtypeStruct((rows, D_REC), F32)
    dup_np = _dup_table()
    return pl.pallas_call(
        body, name="rec_bwd",
        out_shape=(jax.ShapeDtypeStruct((2, T, D_REC), BF16),
                   vec(4), vec(1), wshape, vec(2), wshape, vec(2), vec(2)),
        grid=(N_CG,),
        in_specs=[tok(0), tok(N_CG), tok(0), tok(0), pl.BlockSpec((4, T, CG), lambda g: (0, 0, g)), tok(0),
                  per_ch(4), per_ch(1), wspec, per_ch(2), wspec, per_ch(2), per_ch(2),
                  const((REC_BLOCK, CG)), const((CG, REC_BLOCK)), const((CG, CG))],
        out_specs=(pl.BlockSpec((2, T, CG), lambda g: (0, 0, g)),
                   per_ch(4), per_ch(1), wspec, per_ch(2), wspec, per_ch(2), per_ch(2)),
        scratch_shapes=[pltpu.VMEM((T, CG), F32)] * 5,
        compiler_params=_params(dimension_semantics=("parallel",)),
    )(uy, uy, hf, hb, am, dyrec, conv_w, conv_b, w_a, b_a, w_i, b_i, lam,
      jnp.asarray(dup_np, BF16), jnp.asarray(dup_np.T.copy()), jnp.asarray(_pair_mask()))


TM_MIX = 256


def _mix_specs():
    tok = lambda width, blk=0: pl.BlockSpec((TM_MIX, width), lambda i: (i, blk))
    full = lambda shape: pl.BlockSpec(shape, lambda i: (0, 0))
    return tok, full


def _mix_fwd(x, att, yrec, gg, w_att_o_t, w_rec_o, w_out):
    tok, full = _mix_specs()

    def body(x_ref, att_ref, yr_ref, ga_ref, gr_ref, wao_ref, wro_ref, wo_ref, x1_ref, mixed_ref):
        y_att = _dot_nt(att_ref[...], wao_ref[...])
        y_rec = _dot(yr_ref[...], wro_ref[...])
        mixed = (_sigmoid(ga_ref[...]) * y_att + _sigmoid(gr_ref[...]) * y_rec).astype(BF16)
        mixed_ref[...] = mixed
        x1_ref[...] = x_ref[...] + _dot(mixed, wo_ref[...])

    return pl.pallas_call(
        body, name="mix_fwd",
        out_shape=(jax.ShapeDtypeStruct((T, D), F32), jax.ShapeDtypeStruct((T, D), BF16)),
        grid=(T // TM_MIX,),
        in_specs=[tok(D), tok(D_ATT), tok(D_REC), tok(D, 0), tok(D, 1),
                  full((D, D_ATT)), full((D_REC, D)), full((D, D))],
        out_specs=(tok(D), tok(D)),
        compiler_params=_params(dimension_semantics=("parallel",)),
    )(x, att, yrec, gg, gg, w_att_o_t, w_rec_o, w_out)


def _mix_bwd(dx1, att, yrec, gg, w_att_o_t, w_rec_o, w_out, after):
    tok, full = _mix_specs()

    def body(dx_ref, att_ref, yr_ref, ga_ref, gr_ref, wao_ref, wro_ref, wo_ref, after_ref,
             dgg_ref, dya_ref, dyr_ref, datt_ref, dyrp_ref):
        dmixed = _dot_nt(dx_ref[...].astype(BF16), wo_ref[...])
        y_att = _dot_nt(att_ref[...], wao_ref[...])
        y_rec = _dot(yr_ref[...], wro_ref[...])
        sa = _sigmoid(ga_ref[...])
        sr = _sigmoid(gr_ref[...])
        dgg_ref[0] = (dmixed * y_att * sa * (1.0 - sa)).astype(BF16)
        dgg_ref[1] = (dmixed * y_rec * sr * (1.0 - sr)).astype(BF16)
        dya = (dmixed * sa).astype(BF16)
        dyr = (dmixed * sr).astype(BF16)
        dya_ref[...] = dya
        dyr_ref[...] = dyr
        datt_ref[...] = _dot(dya, wao_ref[...]).astype(BF16)
        dyrp_ref[...] = _dot_nt(dyr, wro_ref[...])

    return pl.pallas_call(
        body, name="mix_bwd",
        out_shape=(jax.ShapeDtypeStruct((2, T, D), BF16),
                   jax.ShapeDtypeStruct((T, D), BF16), jax.ShapeDtypeStruct((T, D), BF16),
                   jax.ShapeDtypeStruct((T, D_ATT), BF16), jax.ShapeDtypeStruct((T, D_REC), F32)),
        grid=(T // TM_MIX,),
        in_specs=[tok(D), tok(D_ATT), tok(D_REC), tok(D, 0), tok(D, 1),
                  full((D, D_ATT)), full((D_REC, D)), full((D, D)), pl.BlockSpec(memory_space=pl.ANY)],
        out_specs=(pl.BlockSpec((2, TM_MIX, D), lambda i: (0, i, 0)),
                   tok(D), tok(D), tok(D_ATT), tok(D_REC)),
        compiler_params=_params(dimension_semantics=("parallel",)),
    )(dx1, att, yrec, gg, gg, w_att_o_t, w_rec_o, w_out, after)


TM_FFN = 256
FF_CHUNK = 1024


def _ffn_loss(x1, target, g2, gf, w_ff1_t, w_ff2):
    n_chunks = D_FF // FF_CHUNK

    def body(x1_ref, tg_ref, g2_ref, gf_ref, w1_hbm, w2_hbm,
             loss_ref, dx1_ref, h2_ref, act_ref, dpre_ref, dx2_ref, dg2_ref, dgf_ref,
             w1, w2, relu_s):
        i = pl.program_id(0)

        @pl.when(i == 0)
        def _():
            pltpu.sync_copy(w1_hbm, w1)
            pltpu.sync_copy(w2_hbm, w2)
            loss_ref[...] = jnp.zeros_like(loss_ref)
            dg2_ref[...] = jnp.zeros_like(dg2_ref)
            dgf_ref[...] = jnp.zeros_like(dgf_ref)

        x1v = x1_ref[...]
        r2 = lax.rsqrt(jnp.mean(x1v * x1v, axis=-1, keepdims=True) + EPS)
        xh2 = x1v * r2
        h2 = (xh2 * g2_ref[...]).astype(BF16)
        h2_ref[...] = h2
        x2 = x1v
        for c in range(n_chunks):
            ff = slice(c * FF_CHUNK, (c + 1) * FF_CHUNK)
            rl = jnp.maximum(_dot_nt(h2, w1[ff, :]), 0.0)
            relu_s[:, ff] = rl
            act = (rl * rl).astype(BF16)
            act_ref[:, ff] = act
            x2 = x2 + _dot(act, w2[ff, :])
        r3 = lax.rsqrt(jnp.mean(x2 * x2, axis=-1, keepdims=True) + EPS)
        xh3 = x2 * r3
        err = xh3 * gf_ref[...] - tg_ref[...]
        loss_ref[...] += 0.5 * jnp.sum(jnp.mean(err * err, axis=-1, keepdims=True))
        dy = err * (1.0 / D)
        dgf_ref[...] += jnp.sum(dy * xh3, axis=0, keepdims=True)
        dx2 = _rms_bwd(dy, xh3, r3, gf_ref[...])
        dx2_16 = dx2.astype(BF16)
        dx2_ref[...] = dx2_16
        dh2 = jnp.zeros((TM_FFN, D), F32)
        for c in range(n_chunks):
            ff = slice(c * FF_CHUNK, (c + 1) * FF_CHUNK)
            dpre = (_dot_nt(dx2_16, w2[ff, :]) * (2.0 * relu_s[:, ff])).astype(BF16)
            dpre_ref[:, ff] = dpre
            dh2 = dh2 + _dot(dpre, w1[ff, :])
        dg2_ref[...] += jnp.sum(dh2 * xh2, axis=0, keepdims=True)
        dx1_ref[...] = dx2 + _rms_bwd(dh2, xh2, r2, g2_ref[...])

    tok = lambda width: pl.BlockSpec((TM_FFN, width), lambda i: (i, 0))
    vec = pl.BlockSpec((1, D), lambda i: (0, 0))
    hbm = pl.BlockSpec(memory_space=pl.ANY)
    return pl.pallas_call(
        body, name="ffn_loss",
        out_shape=(jax.ShapeDtypeStruct((8, 128), F32), jax.ShapeDtypeStruct((T, D), F32),
                   jax.ShapeDtypeStruct((T, D), BF16), jax.ShapeDtypeStruct((T, D_FF), BF16),
                   jax.ShapeDtypeStruct((T, D_FF), BF16), jax.ShapeDtypeStruct((T, D), BF16),
                   jax.ShapeDtypeStruct((1, D), F32), jax.ShapeDtypeStruct((1, D), F32)),
        grid=(T // TM_FFN,),
        in_specs=[tok(D), tok(D), vec, vec, hbm, hbm],
        out_specs=(pl.BlockSpec((8, 128), lambda i: (0, 0)), tok(D), tok(D), tok(D_FF), tok(D_FF), tok(D),
                   vec, vec),
        scratch_shapes=[pltpu.VMEM((D_FF, D), BF16), pltpu.VMEM((D_FF, D), BF16),
                        pltpu.VMEM((TM_FFN, D_FF), F32)],
        compiler_params=_params(dimension_semantics=("arbitrary",)),
    )(x1, target, g2, gf, w_ff1_t, w_ff2)


def _local_step(x, target, p, late_weights, reduce_first, reduce_early):
    bias = _rpb_rows(p["rpb"])
    pairs = lambda w: w.reshape(2, N_CG, CG, REC_BLOCK)
    w_a, w_i = pairs(p["w_rg_a"]), pairs(p["w_rg_i"])
    rec_params = (p["conv_w"], p["conv_b"], w_a, p["b_rg_a"], w_i, p["b_rg_i"], p["lru_lambda"])

    qkv, uy, gg, h = _in_proj(x, p["ln1_g"], p["w_in_t"], p["b_in"])
    att = _att_fwd(qkv, bias)
    hf, hb, yrec, am = _rec_fwd(uy, *rec_params)
    p = {**p, **late_weights(yrec, 0)}
    x1, mixed = _mix_fwd(x, att, yrec, gg, p["w_att_o_t"], p["w_rec_o"], p["w_out"])
    p = {**p, **late_weights(x1, 1)}
    loss8, dx1, h2, act, dpre, dx2, g_ln2, g_lnf = _ffn_loss(
        x1, target, p["ln2_g"], p["lnf_g"], p["w_ff1_t"], p["w_ff2"])

    grads = {"ln2_g": g_ln2, "lnf_g": g_lnf,
             "w_ff1_t": _matmul(dpre, h2, "tn", BF16, "g_w_ff1"),
             "w_ff2": _matmul(act, dx2, "tn", BF16, "g_w_ff2")}
    dgg, dya, dyr, datt, dyrp = _mix_bwd(dx1, att, yrec, gg, p["w_att_o_t"], p["w_rec_o"], p["w_out"],
                                         reduce_first(grads, None))
    lam_after = rec_params[-1] + reduce_first(None, dgg)[0, 0]
    duy, g_cw, g_cb, g_wa, g_ba, g_wi, g_bi, g_lam = _rec_bwd(uy, hf, hb, am, dyrp, *rec_params[:-1], lam_after)
    blocks = lambda g: g.reshape(2, N_REC_BLOCKS, REC_BLOCK, REC_BLOCK)
    grads.update({
        "w_att_o_t": _matmul(dya, att, "tn", BF16, "g_w_att_o"),
        "conv_w": g_cw, "conv_b": g_cb, "w_rg_a": blocks(g_wa), "b_rg_a": g_ba,
        "w_rg_i": blocks(g_wi), "b_rg_i": g_bi, "lru_lambda": g_lam,
        "w_rec_o": _matmul(yrec, dyr, "tn", BF16, "g_w_rec_o"),
        "w_out": _matmul(mixed, dx1, "tn", BF16, "g_w_out"),
    })
    dqkv, gbias = _att_bwd(qkv, bias, datt, reduce_early(grads))
    dz = (dqkv, duy, dgg)
    grad_x, g_ln1 = _dh_norm1_bwd(dz, p["w_in_t"], x, p["ln1_g"], dx1)
    g_w_in_t, g_b_in = _grad_w_in(dz, h)
    grads.update(ln1_g=g_ln1, w_in_t=g_w_in_t, b_in=g_b_in, rpb=_rpb_fold(gbias))
    return loss8[0:1, 0:1], grad_x, grads


MESH_ID = pl.DeviceIdType.MESH
ANY = pl.BlockSpec(memory_space=pl.ANY)

CHAN_BLOCK_ROWS = 32
GATE_ROWS = 2 * 2 * N_REC_BLOCKS * REC_BLOCK * REC_BLOCK // (N_DEV * D)
SECTIONS = (("w_in_t", 704, D), ("w_rec_o", 128, D), ("w_out", 128, D), ("w_ff1_t", 512, D),
            ("w_ff2", 512, D), ("chan", CHAN_BLOCK_ROWS, D), ("w_att_o_t", 128, D_ATT),
            ("gates", GATE_ROWS, D))
N_SEC = len(SECTIONS)
N_CHAN_ROWS = 10
CHAN = (("conv_w", 4), ("b_rg_a", 2), ("b_rg_i", 2), ("lru_lambda", 2))


def _position():
    return lax.axis_index("x"), lax.axis_index("y"), lax.axis_index("c")


def _other_chips(x, y):
    return [(1 - x, y), (x, 1 - y), (1 - x, 1 - y)]


PASS_ON_IDS, PAIR_EARLY_ID, PAIR_LATE_ID, PAIR_FIRST_ID = (1, 4), 2, 3, 5


def _pair_handshake(x, y, c):
    barrier = pltpu.get_barrier_semaphore()
    pl.semaphore_signal(barrier, inc=1, device_id=(x, y, 1 - c), device_id_type=MESH_ID)
    pl.semaphore_wait(barrier, 1)


def _block_of(ref, dev, rows):
    return ref.at[pl.ds(pl.multiple_of(dev * rows, 16), rows)]


def _all_gather(shards, name):
    ns = len(shards)

    def body(*refs):
        x_refs, out_refs, done_ref = refs[:ns], refs[ns:2 * ns], refs[2 * ns]
        send_sems, recv_sems, local_sems = refs[2 * ns + 1:]
        done_ref[0, 0] = 0.0
        x, y, c = _position()
        me, sibling = (x, y, c), (x, y, 1 - c)
        x_nbr, y_nbr, diagonal = _other_chips(x, y)
        north = c == 1
        relay_from = (jnp.where(north, x_nbr[0], y_nbr[0]), jnp.where(north, x_nbr[1], y_nbr[1]))
        relay_to = (jnp.where(north, y_nbr[0], x_nbr[0]), jnp.where(north, y_nbr[1], x_nbr[1]))

        def rows(s, px, py, pc):
            return _block_of(out_refs[s], 4 * px + 2 * py + pc, shards[s].shape[0])

        def copy(k, s, block, to, from_shard=False):
            return pltpu.make_async_remote_copy(
                src_ref=x_refs[s] if from_shard else rows(s, *block), dst_ref=rows(s, *block),
                send_sem=send_sems.at[k * ns + s], recv_sem=recv_sems.at[k * ns + s],
                device_id=to, device_id_type=MESH_ID)

        sections = range(ns)
        mine = [pltpu.make_async_copy(x_refs[s], rows(s, *me), local_sems.at[s]) for s in sections]
        sent = [copy(k, s, me, to, True) for k, to in enumerate((sibling, (*x_nbr, c), (*y_nbr, c)))
                for s in sections]
        for cp in mine + sent:
            cp.start()
        for s in sections:
            copy(1, s, (*x_nbr, c), me).wait_recv()
            copy(2, s, (*y_nbr, c), me).wait_recv()
            sent += [copy(3, s, (*relay_from, c), (*relay_to, c)),
                     copy(4, s, (*x_nbr, c), sibling), copy(5, s, (*y_nbr, c), sibling)]
            for cp in sent[-3:]:
                cp.start()
        for s in sections:
            copy(3, s, (*diagonal, c), me).wait_recv()
            sent.append(copy(6, s, (*diagonal, c), sibling))
            sent[-1].start()
        for s in sections:
            copy(0, s, sibling, me).wait_recv()
            for k, chip in ((4, x_nbr), (5, y_nbr), (6, diagonal)):
                copy(k, s, (*chip, 1 - c), me).wait_recv()
        for cp in sent:
            cp.wait_send()
        for cp in mine:
            cp.wait()

    return pl.pallas_call(
        body, name=name,
        out_shape=tuple(jax.ShapeDtypeStruct((N_DEV * s.shape[0], s.shape[1]), s.dtype) for s in shards)
        + (jax.ShapeDtypeStruct((1, 1), F32),),
        in_specs=[ANY] * ns,
        out_specs=(ANY,) * ns + (pl.BlockSpec(memory_space=pltpu.SMEM),),
        scratch_shapes=[pltpu.SemaphoreType.DMA((7 * ns,)), pltpu.SemaphoreType.DMA((7 * ns,)),
                        pltpu.SemaphoreType.DMA((ns,))],
    )(*shards)


HBM = pl.BlockSpec(memory_space=pltpu.HBM)
SEM = pl.BlockSpec(memory_space=pltpu.SEMAPHORE)
EFFECT = pltpu.SideEffectType.DATAFLOW_SIDE_EFFECTING


def _in_hbm(a):
    return pltpu.with_memory_space_constraint(a, pltpu.HBM)


def _first_hop_copies(shards, x_refs, zones, send_sems, recv_sems):
    ns = len(shards)
    x, y, c = _position()
    targets = [(x, y, 1 - c)] + [(cx, cy, c) for cx, cy in _other_chips(x, y)]
    return [pltpu.make_async_remote_copy(
        src_ref=x_refs[s], dst_ref=_block_of(zones[s], 4 * x + 2 * y + c, shards[s].shape[0]),
        send_sem=send_sems.at[k * ns + s], recv_sem=recv_sems.at[k * ns + s],
        device_id=to, device_id_type=MESH_ID)
        for k, to in enumerate(targets) for s in range(ns)]


def _after_all(arrays, name):
    def body(*refs):
        refs[-1][...] = jnp.zeros_like(refs[-1])

    return pl.pallas_call(
        body, name=name,
        out_shape=jax.ShapeDtypeStruct((8, LANES), F32),
        in_specs=[pl.BlockSpec(memory_space=pl.ANY)] * len(arrays),
        out_specs=pl.BlockSpec(memory_space=pltpu.VMEM),
    )(*arrays)


def _own_blocks_placed(shards, after):
    ns = len(shards)
    x, y, c = _position()
    me = jnp.reshape(4 * x + 2 * y + c, (1,)).astype(jnp.int32)
    shards = [*shards[:-1], shards[-1] + after.astype(shards[-1].dtype)]

    def body(me_ref, *refs):
        for s in range(ns):
            refs[ns + s][...] = refs[s][...]

    return pl.pallas_call(
        body, name="own_blocks_placed",
        out_shape=tuple(jax.ShapeDtypeStruct((N_DEV * s.shape[0], s.shape[1]), s.dtype) for s in shards),
        grid_spec=pltpu.PrefetchScalarGridSpec(
            num_scalar_prefetch=1, grid=(1,),
            in_specs=[pl.BlockSpec(s.shape, lambda i, me: (0, 0)) for s in shards],
            out_specs=tuple(pl.BlockSpec(s.shape, lambda i, me: (me[0], 0)) for s in shards)),
        compiler_params=_params(dimension_semantics=("arbitrary",)),
    )(me, *shards)


def _gather_start(shards, after, name):
    ns = len(shards)
    zones = _own_blocks_placed(shards, after)

    def body(*refs):
        for cp in _first_hop_copies(shards, refs[:ns], refs[ns:2 * ns], refs[2 * ns], refs[2 * ns + 1]):
            cp.start()
        refs[-1][...] = jnp.zeros_like(refs[-1])

    out = pl.pallas_call(
        body, name=name,
        out_shape=(pltpu.SemaphoreType.DMA((4 * ns,)), pltpu.SemaphoreType.DMA((4 * ns,)),
                   *[pltpu.HBM(a.shape, a.dtype) for a in (*shards, *zones)],
                   jax.ShapeDtypeStruct((8, LANES), F32)),
        in_specs=[HBM] * (2 * ns),
        out_specs=(SEM, SEM, *[HBM] * (2 * ns), pl.BlockSpec(memory_space=pltpu.VMEM)),
        input_output_aliases={i: 2 + i for i in range(2 * ns)},
        compiler_params=pltpu.CompilerParams(has_side_effects=EFFECT),
    )(*[_in_hbm(a) for a in shards], *[_in_hbm(a) for a in zones])
    return out[0], out[1], out[2:2 + ns], out[2 + ns:2 + 2 * ns], out[-1]


def _gather_wait(send_sems, recv_sems, shards, zones, which, after, name):
    ns = len(shards)

    def body(*refs):
        copies = _first_hop_copies(shards, refs[:ns], refs[ns:2 * ns], refs[2 * ns], refs[2 * ns + 1])
        for i, cp in enumerate(copies):
            if i % ns in which:
                cp.wait_send()
                cp.wait_recv()

    out = pl.pallas_call(
        body, name=name,
        out_shape=tuple(pltpu.HBM(a.shape, a.dtype) for a in (*shards, *zones)),
        in_specs=[HBM] * (2 * ns) + [SEM, SEM, ANY],
        out_specs=(HBM,) * (2 * ns),
        input_output_aliases={i: i for i in range(2 * ns)},
        compiler_params=pltpu.CompilerParams(has_side_effects=EFFECT),
    )(*shards, *zones, send_sems, recv_sems, after)
    return out[:ns], out[ns:]


def _gather_pass_on(rows, zones, barrier_id, name):
    ns = len(zones)

    def body(*refs):
        in_refs, out_refs = refs[:ns], refs[ns:2 * ns]
        send_sems, recv_sems = refs[2 * ns:]
        x, y, c = _position()
        _pair_handshake(x, y, c)
        copies = [pltpu.make_async_remote_copy(
            src_ref=_block_of(in_refs[s], 4 * cx + 2 * cy + c, rows[s]),
            dst_ref=_block_of(out_refs[s], 4 * cx + 2 * cy + c, rows[s]),
            send_sem=send_sems.at[j * ns + s], recv_sem=recv_sems.at[j * ns + s],
            device_id=(x, y, 1 - c), device_id_type=MESH_ID)
            for j, (cx, cy) in enumerate(_other_chips(x, y)) for s in range(ns)]
        for cp in copies:
            cp.start()
        for cp in copies:
            cp.wait_recv()
        for cp in copies:
            cp.wait_send()

    return pl.pallas_call(
        body, name=name,
        out_shape=tuple(jax.ShapeDtypeStruct(z.shape, z.dtype) for z in zones),
        in_specs=[ANY] * ns, out_specs=(ANY,) * ns,
        input_output_aliases={i: i for i in range(ns)},
        scratch_shapes=[pltpu.SemaphoreType.DMA((3 * ns,)), pltpu.SemaphoreType.DMA((3 * ns,))],
        compiler_params=pltpu.CompilerParams(collective_id=barrier_id),
    )(*zones)


def _pair_copies(sections, g_refs, land, send_sems, recv_sems):
    ns = len(sections)
    x, y, c = _position()
    return [pltpu.make_async_remote_copy(
        src_ref=_block_of(g_refs[s], 2 * k + 1 - c, rows), dst_ref=land[s].at[k],
        send_sem=send_sems.at[k * ns + s], recv_sem=recv_sems.at[k * ns + s],
        device_id=(x, y, 1 - c), device_id_type=MESH_ID)
        for k in range(N_CHIPS) for s, (_, rows, _) in enumerate(sections)]


def _pair_exchange_start(sections, grads, barrier_id, name):
    ns = len(sections)

    def body(*refs):
        _pair_handshake(*_position())
        for cp in _pair_copies(sections, refs[:ns], refs[ns:2 * ns], refs[2 * ns], refs[2 * ns + 1]):
            cp.start()
        refs[-1][...] = jnp.zeros_like(refs[-1])

    zones = [lax.empty((N_CHIPS, rows, cols), BF16) for _, rows, cols in sections]
    n = N_CHIPS * ns
    out = pl.pallas_call(
        body, name=name,
        out_shape=(pltpu.SemaphoreType.DMA((n,)), pltpu.SemaphoreType.DMA((n,)),
                   *[pltpu.HBM(a.shape, a.dtype) for a in (*grads, *zones)],
                   jax.ShapeDtypeStruct((8, LANES), F32)),
        in_specs=[HBM] * (2 * ns),
        out_specs=(SEM, SEM, *[HBM] * (2 * ns), pl.BlockSpec(memory_space=pltpu.VMEM)),
        input_output_aliases={i: 2 + i for i in range(2 * ns)},
        compiler_params=pltpu.CompilerParams(has_side_effects=EFFECT, collective_id=barrier_id),
    )(*[_in_hbm(a) for a in grads], *[_in_hbm(a) for a in zones])
    return out[0], out[1], out[2:2 + ns], out[2 + ns:2 + 2 * ns], out[-1]


def _pair_exchange_wait(sections, send_sems, recv_sems, grads, zones, after, name):
    ns = len(sections)

    def body(*refs):
        for cp in _pair_copies(sections, refs[:ns], refs[ns:2 * ns], refs[2 * ns], refs[2 * ns + 1]):
            cp.wait_send()
            cp.wait_recv()

    out = pl.pallas_call(
        body, name=name,
        out_shape=tuple(pltpu.HBM(a.shape, a.dtype) for a in (*grads, *zones)),
        in_specs=[HBM] * (2 * ns) + [SEM, SEM, ANY],
        out_specs=(HBM,) * (2 * ns),
        input_output_aliases={i: i for i in range(2 * ns)},
        compiler_params=pltpu.CompilerParams(has_side_effects=EFFECT),
    )(*grads, *zones, send_sems, recv_sems, after)
    return out[:ns], out[ns:]


def _pair_add(sections, grads, got, core, name):
    ns = len(sections)

    def body(core_ref, *refs):
        g_refs, got_refs, p_refs = refs[:ns], refs[ns:2 * ns], refs[2 * ns:]
        for s in range(ns):
            p_refs[s][0] = (g_refs[s][...].astype(F32) + got_refs[s][0].astype(F32)).astype(BF16)

    slot = [pl.BlockSpec((1, rows, cols), lambda k, c: (k, 0, 0)) for _, rows, cols in sections]
    return pl.pallas_call(
        body, name=name,
        out_shape=tuple(jax.ShapeDtypeStruct((N_CHIPS, rows, cols), BF16) for _, rows, cols in sections),
        grid_spec=pltpu.PrefetchScalarGridSpec(
            num_scalar_prefetch=1, grid=(N_CHIPS,),
            in_specs=[pl.BlockSpec((rows, cols), lambda k, c: (2 * k + c[0], 0)) for _, rows, cols in sections]
            + slot,
            out_specs=tuple(slot)),
        compiler_params=_params(dimension_semantics=("parallel",)),
    )(core, *grads, *got)


def _chip_copies(sections, p_refs, land, send_sems, recv_sems):
    ns = len(sections)
    x, y, c = _position()
    return [pltpu.make_async_remote_copy(
        src_ref=p_refs[s].at[2 * cx + cy], dst_ref=land[s].at[j],
        send_sem=send_sems.at[j * ns + s], recv_sem=recv_sems.at[j * ns + s],
        device_id=(cx, cy, c), device_id_type=MESH_ID)
        for j, (cx, cy) in enumerate(_other_chips(x, y)) for s in range(ns)]


def _chip_exchange(sections, parts, name):
    ns = len(sections)

    def body(*refs):
        copies = _chip_copies(sections, refs[:ns], refs[ns:2 * ns], *refs[2 * ns:])
        for cp in copies:
            cp.start()
        for cp in copies:
            cp.wait_recv()
        for cp in copies:
            cp.wait_send()

    n = 3 * ns
    return pl.pallas_call(
        body, name=name,
        out_shape=tuple(jax.ShapeDtypeStruct((3, rows, cols), BF16) for _, rows, cols in sections),
        in_specs=[ANY] * ns, out_specs=(ANY,) * ns,
        scratch_shapes=[pltpu.SemaphoreType.DMA((n,)), pltpu.SemaphoreType.DMA((n,))],
    )(*parts)


def _chip_exchange_start(sections, parts, name):
    ns = len(sections)

    def body(*refs):
        p_refs, land = refs[:ns], refs[ns:2 * ns]
        send_sems, recv_sems = refs[2 * ns], refs[2 * ns + 1]
        token = refs[-1]
        for cp in _chip_copies(sections, p_refs, land, send_sems, recv_sems):
            cp.start()
        token[...] = jnp.zeros_like(token)

    zones = [lax.empty((3, rows, cols), BF16) for _, rows, cols in sections]
    out = pl.pallas_call(
        body, name=name,
        out_shape=(pltpu.SemaphoreType.DMA((3 * ns,)), pltpu.SemaphoreType.DMA((3 * ns,)),
                   *[pltpu.HBM(a.shape, a.dtype) for a in parts], *[pltpu.HBM(a.shape, a.dtype) for a in zones],
                   jax.ShapeDtypeStruct((8, LANES), F32)),
        in_specs=[HBM] * (2 * ns),
        out_specs=(SEM, SEM, *[HBM] * (2 * ns), pl.BlockSpec(memory_space=pltpu.VMEM)),
        input_output_aliases={i: 2 + i for i in range(2 * ns)},
        compiler_params=pltpu.CompilerParams(has_side_effects=EFFECT),
    )(*[_in_hbm(a) for a in parts], *[_in_hbm(a) for a in zones])
    return out[0], out[1], out[2:2 + ns], out[2 + ns:2 + 2 * ns], out[-1]


def _chip_exchange_wait(sections, send_sems, recv_sems, parts, zones, after, name):
    ns = len(sections)

    def body(*refs):
        p_refs, land = refs[:ns], refs[ns:2 * ns]
        for cp in _chip_copies(sections, p_refs, land, refs[2 * ns], refs[2 * ns + 1]):
            cp.wait_send()
            cp.wait_recv()

    out = pl.pallas_call(
        body, name=name,
        out_shape=tuple(pltpu.HBM(a.shape, a.dtype) for a in (*parts, *zones)),
        in_specs=[HBM] * (2 * ns) + [SEM, SEM, ANY],
        out_specs=(HBM,) * (2 * ns),
        input_output_aliases={i: i for i in range(2 * ns)},
        compiler_params=pltpu.CompilerParams(has_side_effects=EFFECT),
    )(*parts, *zones, send_sems, recv_sems, after)
    return out[:ns], out[ns:]


def _grad_finish(sections, parts, far, chip, name):
    ns = len(sections)

    def body(chip_ref, *refs):
        p_refs, b_refs, g_refs = refs[:ns], refs[ns:2 * ns], refs[2 * ns:]
        for s in range(ns):
            g = p_refs[s][0].astype(F32)
            for j in range(3):
                g = g + b_refs[s][j].astype(F32)
            g_refs[s][...] = g

    half = [(rows // 2, cols) for _, rows, cols in sections]
    return pl.pallas_call(
        body, name=name,
        out_shape=tuple(jax.ShapeDtypeStruct((rows, cols), F32) for _, rows, cols in sections),
        grid_spec=pltpu.PrefetchScalarGridSpec(
            num_scalar_prefetch=1, grid=(2,),
            in_specs=[pl.BlockSpec((1, r, c), lambda i, chip: (chip[0], i, 0)) for r, c in half]
            + [pl.BlockSpec((3, r, c), lambda i, chip: (0, i, 0)) for r, c in half],
            out_specs=tuple(pl.BlockSpec((r, c), lambda i, chip: (i, 0)) for r, c in half)),
        compiler_params=_params(dimension_semantics=("parallel",)),
    )(chip, *parts, *far)


def _sum_devices(parts, rows, name):
    cols = parts.shape[1]
    tr = rows // 2

    def body(*refs):
        s = refs[0][...].astype(F32)
        for d in range(1, N_DEV):
            s = s + refs[d][...].astype(F32)
        refs[N_DEV][...] = s

    return pl.pallas_call(
        body, name=name,
        out_shape=jax.ShapeDtypeStruct((rows, cols), F32),
        grid=(2,),
        in_specs=[pl.BlockSpec((tr, cols), lambda i, d=d: (2 * d + i, 0)) for d in range(N_DEV)],
        out_specs=pl.BlockSpec((tr, cols), lambda i: (i, 0)),
        compiler_params=_params(dimension_semantics=("parallel",)),
    )(*([parts] * N_DEV))


def _adamw_step(w_ref, g_ref, m_ref, v_ref, d_ref, nm_ref, nv_ref):
    c1 = 1.0 / (1.0 - ADAM_B1 ** ADAM_STEP)
    c2 = 1.0 / (1.0 - ADAM_B2 ** ADAM_STEP)
    gv = g_ref[...]
    nm = ADAM_B1 * m_ref[...] + (1.0 - ADAM_B1) * gv
    nv = ADAM_B2 * v_ref[...] + (1.0 - ADAM_B2) * (gv * gv)
    nm_ref[...] = nm
    nv_ref[...] = nv
    d_ref[...] = (-ADAM_LR) * ((nm * c1) / (jnp.sqrt(nv * c2) + ADAM_EPS) + ADAM_WD * w_ref[...])


def _adamw_small(params, name):
    n = len(params)

    def body(*refs):
        for k in range(n):
            _adamw_step(*refs[4 * k:4 * k + 4], *refs[4 * n + 3 * k:4 * n + 3 * k + 3])

    out = pl.pallas_call(
        body, name=name,
        out_shape=tuple(jax.ShapeDtypeStruct(p[0].shape, F32) for p in params for _ in range(3)),
    )(*[a for p in params for a in p])
    return [out[3 * k:3 * k + 3] for k in range(n)]


def _adamw(w, g, m, v, name):
    rows, cols = w.shape
    tr = rows
    while tr * cols * 4 > (1 << 20) and tr % 16 == 0:
        tr //= 2

    def body(*refs):
        _adamw_step(*refs)

    spec = pl.BlockSpec((tr, cols), lambda i: (i, 0))
    shape = jax.ShapeDtypeStruct((rows, cols), F32)
    return pl.pallas_call(
        body, name=name,
        out_shape=(shape, shape, shape),
        grid=(rows // tr,),
        in_specs=[spec] * 4, out_specs=(spec,) * 3,
        compiler_params=_params(dimension_semantics=("parallel",)),
    )(w, g, m, v)


NAMES = ("ln1_g", "w_in", "b_in", "rpb", "w_att_o", "conv_w", "conv_b", "w_rg_a", "b_rg_a", "w_rg_i",
         "b_rg_i", "lru_lambda", "w_rec_o", "w_out", "ln2_g", "w_ff1", "w_ff2", "lnf_g")
TRANSPOSED = {"w_in": "w_in_t", "w_att_o": "w_att_o_t", "w_ff1": "w_ff1_t"}
ROW_SHARDED = ("w_rec_o", "w_out", "w_ff2")
REPLICATED = (("ln1_g", (1, D)), ("b_in", (1, D_IN)), ("rpb", (N_HEADS * N_RPB_R, N_RPB_C)),
              ("conv_b", (1, D_REC)), ("w_rg_a", (2 * N_REC_BLOCKS * REC_BLOCK, REC_BLOCK)),
              ("w_rg_i", (2 * N_REC_BLOCKS * REC_BLOCK, REC_BLOCK)), ("ln2_g", (1, D)), ("lnf_g", (1, D)))
GATE_BLOCKS = ("w_rg_a", "w_rg_i")
SMALL_ROWS = 112


def _chan_bits(vectors):
    chan = jnp.concatenate(vectors, axis=0)
    bits = lax.bitcast_convert_type(chan, BF16).reshape(-1)
    return jnp.pad(bits, (0, CHAN_BLOCK_ROWS * D - bits.shape[0])).reshape(CHAN_BLOCK_ROWS, D)


def _chan_from_bits(gathered):
    bits = gathered.reshape(N_DEV, CHAN_BLOCK_ROWS * D)[:, :2 * N_CHAN_ROWS * LANES]
    chan = lax.bitcast_convert_type(bits.reshape(N_DEV, N_CHAN_ROWS, LANES, 2), F32)
    return chan.transpose(1, 0, 2).reshape(N_CHAN_ROWS, D)


def kernel(x, ln1_g, w_in, b_in, rpb, w_att_o, conv_w, conv_b, w_rg_a, b_rg_a, w_rg_i, b_rg_i, lru_lambda, w_rec_o, w_out, ln2_g, w_ff1, w_ff2, lnf_g, loss_target, m_ln1_g, m_w_in, m_b_in, m_rpb, m_w_att_o, m_conv_w, m_conv_b, m_w_rg_a, m_b_rg_a, m_w_rg_i, m_b_rg_i, m_lru_lambda, m_w_rec_o, m_w_out, m_ln2_g, m_w_ff1, m_w_ff2, m_lnf_g, v_ln1_g, v_w_in, v_b_in, v_rpb, v_w_att_o, v_conv_w, v_conv_b, v_w_rg_a, v_b_rg_a, v_w_rg_i, v_b_rg_i, v_lru_lambda, v_w_rec_o, v_w_out, v_ln2_g, v_w_ff1, v_w_ff2, v_lnf_g):
    w = dict(zip(NAMES, (ln1_g, w_in, b_in, rpb, w_att_o, conv_w, conv_b, w_rg_a, b_rg_a, w_rg_i,
                         b_rg_i, lru_lambda, w_rec_o, w_out, ln2_g, w_ff1, w_ff2, lnf_g)))
    m = dict(zip(NAMES, (m_ln1_g, m_w_in, m_b_in, m_rpb, m_w_att_o, m_conv_w, m_conv_b, m_w_rg_a,
                         m_b_rg_a, m_w_rg_i, m_b_rg_i, m_lru_lambda, m_w_rec_o, m_w_out, m_ln2_g,
                         m_w_ff1, m_w_ff2, m_lnf_g)))
    v = dict(zip(NAMES, (v_ln1_g, v_w_in, v_b_in, v_rpb, v_w_att_o, v_conv_w, v_conv_b, v_w_rg_a,
                         v_b_rg_a, v_w_rg_i, v_b_rg_i, v_lru_lambda, v_w_rec_o, v_w_out, v_ln2_g,
                         v_w_ff1, v_w_ff2, v_lnf_g)))
    xi, yi, ci = _position()

    shard = {t: w[n][0].T.astype(BF16) for n, t in TRANSPOSED.items()}
    shard.update({n: w[n][0].astype(BF16) for n in ROW_SHARDED})
    shard["chan"] = _chan_bits([w[n][0] for n, _ in CHAN])
    first, later = ("w_in_t", "chan"), ("w_rec_o", "w_out", "w_att_o_t", "w_ff1_t", "w_ff2")
    *gathered, done = _all_gather([shard[n] for n in first], "weight_all_gather")
    p = dict(zip(first, gathered))
    send_sems, recv_sems, sent, zones, token = _gather_start([shard[n] for n in later], done,
                                                             "weight_gather_start")

    travelling = {"shards": sent, "zones": zones}
    stages = (("w_rec_o", "w_out", "w_att_o_t"), ("w_ff1_t", "w_ff2"))

    def late_weights(after, stage):
        which = [later.index(n) for n in stages[stage]]
        travelling["shards"], travelling["zones"] = _gather_wait(
            send_sems, recv_sems, travelling["shards"], travelling["zones"], which, after,
            "weight_gather_wait_%d" % stage)
        return dict(zip(stages[stage], _gather_pass_on(
            [shard[n].shape[0] for n in stages[stage]], [travelling["zones"][i] for i in which],
            PASS_ON_IDS[stage], "weight_gather_pass_on_%d" % stage)))

    chan = _chan_from_bits(p.pop("chan"))
    r0 = 0
    for n, rows in CHAN:
        p[n] = chan[r0:r0 + rows]
        r0 += rows
    p.update(ln1_g=w["ln1_g"], b_in=w["b_in"] + token[0, 0], rpb=w["rpb"][0], conv_b=w["conv_b"],
             w_rg_a=w["w_rg_a"][0], w_rg_i=w["w_rg_i"][0], ln2_g=w["ln2_g"],
             lnf_g=w["lnf_g"].reshape(1, D))

    core = jnp.reshape(ci, (1,)).astype(jnp.int32)
    chip = jnp.reshape(2 * xi + yi, (1,)).astype(jnp.int32)
    first_sections = tuple(s for s in SECTIONS if s[0] in ("w_ff1_t", "w_ff2"))
    late_sections = SECTIONS[:1]
    early_sections = tuple(s for s in SECTIONS[1:] if s not in first_sections)
    in_flight = {}

    def pair_sum_and_send(group, sections, after):
        send_sems, recv_sems, sect, zones, _ = in_flight["pair_" + group]
        sect, got = _pair_exchange_wait(sections, send_sems, recv_sems, sect, zones, after,
                                        "grad_pair_exchange_wait_" + group)
        parts = _pair_add(sections, sect, got, core, "grad_pair_add_" + group)
        in_flight[group] = _chip_exchange_start(sections, parts, "grad_chip_exchange_start_" + group)
        return in_flight[group][-1]

    def pair_exchange_at_once(group, sections, grads, barrier_id):
        in_flight["pair_" + group] = _pair_exchange_start(
            sections, [grads[n] for n, _, _ in sections], barrier_id, "grad_pair_exchange_start_" + group)
        return pair_sum_and_send(group, sections, in_flight["pair_" + group][-1])

    def reduce_first(grads, after):
        if grads is None:
            return pair_sum_and_send("first", first_sections, after)
        in_flight["pair_first"] = _pair_exchange_start(
            first_sections, [grads[n] for n, _, _ in first_sections], PAIR_FIRST_ID,
            "grad_pair_exchange_start_first")
        return in_flight["pair_first"][-1]

    def reduce_early(grads):
        chan_g = jnp.concatenate([grads[n] for n, _ in CHAN], axis=0)
        chan_g = chan_g.reshape(N_CHAN_ROWS, N_DEV, LANES).transpose(1, 0, 2).astype(BF16)
        chan_g = jnp.pad(chan_g.reshape(N_DEV, -1), ((0, 0), (0, CHAN_BLOCK_ROWS * D - N_CHAN_ROWS * LANES)))
        grads["chan"] = chan_g.reshape(N_DEV * CHAN_BLOCK_ROWS, D)
        grads["gates"] = jnp.concatenate([grads[n].reshape(-1, D) for n in GATE_BLOCKS], axis=0).astype(BF16)
        return pair_exchange_at_once("early", early_sections, grads, PAIR_EARLY_ID)[0, 0]

    loss_part, grad_x, grads = _local_step(x[0], loss_target[0], p, late_weights, reduce_first, reduce_early)
    in_flight["pair_late"] = _pair_exchange_start(
        late_sections, [grads[n] for n, _, _ in late_sections], PAIR_LATE_ID, "grad_pair_exchange_start_late")

    def finish(group, sections, after, name):
        send_sems, recv_sems, parts, zones, _ = in_flight[group]
        parts, far = _chip_exchange_wait(sections, send_sems, recv_sems, parts, zones, after,
                                         "grad_chip_exchange_wait_" + name)
        return dict(zip((n for n, _, _ in sections),
                        _grad_finish(sections, parts, far, chip, "grad_finish_" + name)))

    summed = finish("first", first_sections, in_flight["pair_late"][-1], "first")
    summed.update(finish("early", early_sections, summed["w_ff2"], "early"))
    started_late = pair_sum_and_send("late", late_sections, summed["gates"])

    flat = jnp.concatenate([grads[n].reshape(-1) for n, _ in REPLICATED if n not in GATE_BLOCKS]
                           + [loss_part.reshape(-1) + started_late[0, 0]])
    n_small = flat.shape[0]
    flat = jnp.pad(flat, (0, SMALL_ROWS * LANES - n_small)).reshape(SMALL_ROWS, LANES)
    small_parts, gate_sum, _ = _all_gather([flat, summed["gates"]], "small_grad_all_gather")
    small = _sum_devices(small_parts, SMALL_ROWS, "small_grad_sum").reshape(-1)
    loss = small[n_small - 1]

    g, delta, new_m, new_v = {}, {}, {}, {}

    def update(n, g2, shape2):
        d2, m2, v2 = _adamw(w[n].reshape(shape2), g2, m[n].reshape(shape2), v[n].reshape(shape2),
                            "adamw_" + n)
        g[n], delta[n], new_m[n], new_v[n] = (a.reshape(w[n].shape) for a in (g2, d2, m2, v2))

    small_params = []
    o = 0
    for n, shape2 in REPLICATED:
        if n in GATE_BLOCKS:
            k, rows = GATE_BLOCKS.index(n), gate_sum.shape[0] // len(GATE_BLOCKS)
            update(n, gate_sum[k * rows:(k + 1) * rows].reshape(shape2), shape2)
        else:
            size = shape2[0] * shape2[1]
            small_params.append((n, small[o:o + size].reshape(shape2), shape2))
            o += size
    chan_back = summed["chan"].reshape(-1)[:N_CHAN_ROWS * LANES].reshape(N_CHAN_ROWS, LANES)
    r0 = 0
    for n, rows in CHAN:
        small_params.append((n, chan_back[r0:r0 + rows], (rows, LANES)))
        r0 += rows
    results = _adamw_small([(w[n].reshape(s2), g2, m[n].reshape(s2), v[n].reshape(s2))
                            for n, g2, s2 in small_params], "adamw_vectors")
    for (n, g2, _), (d2, m2, v2) in zip(small_params, results):
        g[n], delta[n], new_m[n], new_v[n] = (a.reshape(w[n].shape) for a in (g2, d2, m2, v2))

    for n in ROW_SHARDED:
        update(n, summed[n], summed[n].shape)
    for n, t in TRANSPOSED.items():
        if t in summed:
            update(n, summed[t].T, summed[t].shape[::-1])
    summed = finish("late", late_sections, _after_all(list(delta.values()), "updates_done"), "late")
    update("w_in", summed["w_in_t"].T, summed["w_in_t"].shape[::-1])

    return (loss, grad_x[None], *[g[n] for n in NAMES], *[delta[n] for n in NAMES],
            *[new_m[n] for n in NAMES], *[new_v[n] for n in NAMES])
```

```python
import math

import numpy as np
import jax
import jax.numpy as jnp
from jax import lax
from jax.experimental import pallas as pl
from jax.experimental.pallas import tpu as pltpu

F32 = jnp.float32
BF16 = jnp.bfloat16

T = 2048
D = 1024
D_ATT = 512
D_REC = 1024
D_FF = 4096
D_IN = 5632
N_HEADS = 8
DH = 64
GRID_W = 64
ROWS = T // GRID_W
WIN_H = 8
WIN_W = 16
KWIN = WIN_H * GRID_W
N_RPB_R = 2 * WIN_H - 1
N_RPB_C = 2 * WIN_W - 1
N_REC_BLOCKS = 16
REC_BLOCK = 64
CG = 128
N_CG = D_REC // CG
LRU_C = 8.0
EPS = 1e-6
N_DEV = 8
N_CHIPS = 4
LANES = 128

ADAM_LR = 0.001
ADAM_B1 = 0.9
ADAM_B2 = 0.999
ADAM_EPS = 1e-08
ADAM_WD = 0.01
ADAM_STEP = 10

MESH_AXES = ("x", "y", "c")
VMEM_LIMIT = 56 * 1024 * 1024

TILE = 512
DZ_ARRAYS = ((0, 3, 1), (3, 4, 2), (7, 4, 2))
N_DZ_TILES = D_IN // TILE


def _params(**kw):
    return pltpu.CompilerParams(vmem_limit_bytes=VMEM_LIMIT, **kw)


HG = 4
HQ = HG * GRID_W
HC = HG * DH


def _att_tables():
    rq = np.arange(GRID_W)
    kc = np.arange(KWIN) % GRID_W
    win_start = np.clip(rq - WIN_W // 2, 0, GRID_W - WIN_W)
    valid = (kc[None, :] >= win_start[:, None]) & (kc[None, :] < win_start[:, None] + WIN_W)
    same_head = (np.arange(HQ)[:, None] // GRID_W) == (np.arange(HC)[None, :] // DH)
    return valid.astype(np.float32), same_head.astype(np.float32)


def _pair_mask():
    half = np.arange(2 * DH) // DH
    return (half[:, None] == half[None, :]).astype(np.float32)


def _dup_table():
    return np.concatenate([np.eye(REC_BLOCK, dtype=np.float32)] * 2, axis=1)


def _sigmoid(x):
    return 0.5 * jnp.tanh(0.5 * x) + 0.5


def _softplus(x):
    return jnp.maximum(x, 0.0) + jnp.log(1.0 + jnp.exp(-jnp.abs(x)))


def _one_minus_square(log_a, a):
    x = 2.0 * log_a
    series = -x * (1.0 + x * (0.5 + x * (1.0 / 6.0)))
    return jnp.where(x > -0.02, series, 1.0 - a * a)


_GELU_C = math.sqrt(2.0 / math.pi)


def _gelu_and_grad(x):
    x2 = x * x
    inner = _GELU_C * (x + 0.044715 * x * x2)
    t = jnp.tanh(inner)
    g = 0.5 * x * (1.0 + t)
    dg = 0.5 * (1.0 + t) + 0.5 * x * (1.0 - t * t) * _GELU_C * (1.0 + 3.0 * 0.044715 * x2)
    return g, dg


def _dot(a, b):
    return jnp.dot(a, b, preferred_element_type=F32)


def _dot_nt(a, b):
    return lax.dot_general(a, b, (((1,), (1,)), ((), ())), preferred_element_type=F32)


def _dot_tn(a, b):
    return lax.dot_general(a, b, (((0,), (0,)), ((), ())), preferred_element_type=F32)


def _dot_exact(a, b):
    return jnp.dot(a, b, precision=lax.Precision.HIGHEST, preferred_element_type=F32)


def _shift_rows(x, s):
    n = x.shape[0]
    rows = lax.broadcasted_iota(jnp.int32, x.shape, 0)
    y = pltpu.roll(x, s % n, 0)
    if s > 0:
        return jnp.where(rows >= s, y, 0.0)
    return jnp.where(rows < n + s, y, 0.0)


def _rms_bwd(dh, xh, r, g):
    dxh = dh * g
    return r * (dxh - xh * jnp.mean(dxh * xh, axis=-1, keepdims=True))


def _matmul(a, b, mode, out_dtype, name, tm=512, tn=1024, tk=2048):
    if mode == "nn":
        (m, k), (k2, n) = a.shape, b.shape
    elif mode == "nt":
        (m, k), (n, k2) = a.shape, b.shape
    else:
        (k, m), (k2, n) = a.shape, b.shape
    assert k == k2
    tm, tn, tk = min(tm, m), min(tn, n), min(tk, k)
    assert m % tm == 0 and n % tn == 0 and k % tk == 0
    nk = k // tk
    dot = {"nn": _dot, "nt": _dot_nt, "tn": _dot_tn}[mode]

    def body(a_ref, b_ref, o_ref, acc):
        kk = pl.program_id(2)
        part = dot(a_ref[...].astype(BF16), b_ref[...].astype(BF16))
        if nk == 1:
            o_ref[...] = part.astype(out_dtype)
            return

        @pl.when(kk == 0)
        def _():
            acc[...] = part

        @pl.when(kk > 0)
        def _():
            acc[...] += part

        @pl.when(kk == nk - 1)
        def _():
            o_ref[...] = acc[...].astype(out_dtype)

    if mode == "tn":
        a_spec = pl.BlockSpec((tk, tm), lambda i, j, kk: (kk, i))
    else:
        a_spec = pl.BlockSpec((tm, tk), lambda i, j, kk: (i, kk))
    if mode == "nt":
        b_spec = pl.BlockSpec((tn, tk), lambda i, j, kk: (j, kk))
    else:
        b_spec = pl.BlockSpec((tk, tn), lambda i, j, kk: (kk, j))
    return pl.pallas_call(
        body, name=name,
        out_shape=jax.ShapeDtypeStruct((m, n), out_dtype),
        grid=(m // tm, n // tn, nk),
        in_specs=[a_spec, b_spec],
        out_specs=pl.BlockSpec((tm, tn), lambda i, j, kk: (i, j)),
        scratch_shapes=[pltpu.VMEM((tm, tn) if nk > 1 else (8, LANES), F32)],
        compiler_params=_params(dimension_semantics=("parallel", "parallel", "arbitrary")),
    )(a, b)


def _in_proj(x, g1, w_in_t, b_in):
    tm = 512

    def body(x_ref, g_ref, w_hbm, b_ref, qkv_ref, uy_ref, gg_ref, h_ref, w):
        @pl.when(pl.program_id(0) == 0)
        def _():
            pltpu.sync_copy(w_hbm, w)

        xv = x_ref[...]
        r = lax.rsqrt(jnp.mean(xv * xv, axis=-1, keepdims=True) + EPS)
        h = ((xv * r) * g_ref[...]).astype(BF16)
        h_ref[...] = h
        row0 = 0
        for ref in (qkv_ref, uy_ref, gg_ref):
            for c0 in range(0, ref.shape[1], TILE):
                z = _dot_nt(h, w[row0:row0 + TILE, :]) + b_ref[:, row0:row0 + TILE]
                ref[:, c0:c0 + TILE] = z.astype(ref.dtype)
                row0 += TILE

    tok = lambda width: pl.BlockSpec((tm, width), lambda i: (i, 0))
    return pl.pallas_call(
        body, name="in_proj",
        out_shape=(jax.ShapeDtypeStruct((T, 3 * D_ATT), BF16),
                   jax.ShapeDtypeStruct((T, 2 * D_REC), F32),
                   jax.ShapeDtypeStruct((T, 2 * D), F32),
                   jax.ShapeDtypeStruct((T, D), BF16)),
        grid=(T // tm,),
        in_specs=[tok(D), pl.BlockSpec((1, D), lambda i: (0, 0)), pl.BlockSpec(memory_space=pl.ANY),
                  pl.BlockSpec((1, D_IN), lambda i: (0, 0))],
        out_specs=(tok(3 * D_ATT), tok(2 * D_REC), tok(2 * D), tok(D)),
        scratch_shapes=[pltpu.VMEM((D_IN, D), BF16)],
        compiler_params=_params(dimension_semantics=("arbitrary",)),
    )(x, g1, w_in_t, b_in)


def _dz_specs(rows, tile_of, row_of):
    def spec(off, n, per_plane):
        def index(*ids):
            t = jnp.clip(tile_of(*ids) - off, 0, n - 1)
            return (t // per_plane, row_of(*ids), t % per_plane)
        return pl.BlockSpec((1, rows, TILE), index)
    return [spec(off, n, per) for off, n, per in DZ_ARRAYS]


def _dh_norm1_bwd(dz, w_in_t, x, g1, dx1):
    tm = 512

    def body(dqkv_ref, duy_ref, dgg_ref, w_hbm, x_ref, g_ref, dx1_ref, gx_ref, dg_ref, w):
        @pl.when(pl.program_id(0) == 0)
        def _():
            pltpu.sync_copy(w_hbm, w)
            dg_ref[...] = jnp.zeros_like(dg_ref)

        dh, row0 = None, 0
        for ref in (dqkv_ref, duy_ref, dgg_ref):
            for plane in range(ref.shape[0]):
                cols = ref.shape[2]
                part = _dot(ref[plane], w[row0:row0 + cols, :])
                dh = part if dh is None else dh + part
                row0 += cols
        xv = x_ref[...]
        r = lax.rsqrt(jnp.mean(xv * xv, axis=-1, keepdims=True) + EPS)
        xh = xv * r
        dg_ref[...] += jnp.sum(dh * xh, axis=0, keepdims=True)
        gx_ref[...] = dx1_ref[...] + _rms_bwd(dh, xh, r, g_ref[...])

    tok = pl.BlockSpec((tm, D), lambda i: (i, 0))
    vec = pl.BlockSpec((1, D), lambda i: (0, 0))
    planes = lambda a: pl.BlockSpec((a.shape[0], tm, a.shape[2]), lambda i: (0, i, 0))
    return pl.pallas_call(
        body, name="dh_norm1_bwd",
        out_shape=(jax.ShapeDtypeStruct((T, D), F32), jax.ShapeDtypeStruct((1, D), F32)),
        grid=(T // tm,),
        in_specs=[planes(a) for a in dz] + [pl.BlockSpec(memory_space=pl.ANY), tok, vec, tok],
        out_specs=(tok, vec),
        scratch_shapes=[pltpu.VMEM((D_IN, D), BF16)],
        compiler_params=_params(dimension_semantics=("arbitrary",)),
    )(*dz, w_in_t, x, g1, dx1)


def _grad_w_in(dz, h):
    def body(*refs):
        seg_refs = refs[:3]
        h_ref, gw_ref, gb_ref = refs[3:]
        j = pl.program_id(0)

        for s, (off, n, _) in enumerate(DZ_ARRAYS):
            @pl.when((j >= off) & (j < off + n))
            def _(s=s):
                a = seg_refs[s][0]
                gw_ref[...] = _dot_tn(a, h_ref[...]).astype(BF16)
                gb_ref[...] = jnp.sum(a.astype(F32), axis=0, keepdims=True)

    return pl.pallas_call(
        body, name="grad_w_in",
        out_shape=(jax.ShapeDtypeStruct((D_IN, D), BF16), jax.ShapeDtypeStruct((1, D_IN), F32)),
        grid=(N_DZ_TILES,),
        in_specs=_dz_specs(T, lambda j: j, lambda j: 0) + [pl.BlockSpec((T, D), lambda j: (0, 0))],
        out_specs=(pl.BlockSpec((TILE, D), lambda j: (j, 0)), pl.BlockSpec((1, TILE), lambda j: (0, j))),
        compiler_params=_params(dimension_semantics=("parallel",)),
    )(*dz, h)


def _rpb_rows(rpb):
    padded = jnp.pad(rpb, ((0, 0), (0, 0), (0, GRID_W - N_RPB_C)))
    rows = [padded[:, WIN_H - 1 - oi: 2 * WIN_H - 1 - oi].reshape(N_HEADS // HG, HG, KWIN)
            for oi in range(WIN_H)]
    return jnp.stack(rows, axis=0)


SKEW = KWIN - (WIN_W - 1)


MASKED = -1e30


def _bias_tiles(rows_ref, valid, bias_s):
    for oi in range(WIN_H):
        for hh in range(HG):
            row = jnp.broadcast_to(rows_ref[oi, 0, hh:hh + 1, :], (GRID_W, KWIN))
            tile = pltpu.roll(row, SKEW, 1, stride=1, stride_axis=0)
            bias_s[oi, hh * GRID_W:(hh + 1) * GRID_W, :] = jnp.where(valid, tile, MASKED)


def _bias_tile_grads(gb_s, flip, out_ref):
    for oi in range(WIN_H):
        for hh in range(HG):
            g = _dot_exact(flip, gb_s[oi, hh * GRID_W:(hh + 1) * GRID_W, :])
            back = pltpu.roll(g, KWIN - (GRID_W - WIN_W), 1, stride=1, stride_axis=0)
            out_ref[0, oi, hh:hh + 1, :] = jnp.sum(back, axis=0, keepdims=True)


def _rpb_fold(row_grads):
    g = row_grads.transpose(1, 0, 2, 3).reshape(WIN_H, N_HEADS, WIN_H, GRID_W)
    g = g.transpose(0, 2, 1, 3)

    def body(g_ref, o_ref):
        for dr in range(N_RPB_R):
            terms = [g_ref[oi, i] for oi in range(WIN_H) for i in range(WIN_H) if i - oi + WIN_H - 1 == dr]
            acc = terms[0]
            for term in terms[1:]:
                acc = acc + term
            o_ref[dr] = acc

    out = pl.pallas_call(
        body, name="rpb_fold",
        out_shape=jax.ShapeDtypeStruct((N_RPB_R, N_HEADS, GRID_W), F32),
    )(g)
    return out.transpose(1, 0, 2)[:, :, :N_RPB_C]


ATT_GROUPS = N_HEADS // HG
ATT_UNROLL = 8


def _stacked(rows64, same_head):
    return jnp.where(same_head, jnp.concatenate([rows64] * HG, axis=0), jnp.zeros((), BF16))


def _own_heads(stacked):
    head = lax.broadcasted_iota(jnp.int32, (GRID_W, HC), 1) // DH
    out = stacked[:GRID_W]
    for h in range(1, HG):
        out = jnp.where(head == h, stacked[h * GRID_W:(h + 1) * GRID_W], out)
    return out


def _att_scores(q_ref, k_ref, bias_ref, same_head, r):
    rs = jnp.clip(r - WIN_H // 2, 0, ROWS - WIN_H)
    oi = r - rs
    q0 = pl.multiple_of(r * GRID_W, GRID_W)
    k0 = pl.multiple_of(rs * GRID_W, GRID_W)
    q2 = _stacked(q_ref[pl.ds(q0, GRID_W), :] * (DH ** -0.5), same_head)
    kw = k_ref[pl.ds(k0, KWIN), :]
    s = _dot_nt(q2, kw) + bias_ref[oi]
    e = jnp.exp(s - jnp.max(s, axis=-1, keepdims=True))
    return e, 1.0 / jnp.sum(e, axis=-1, keepdims=True), q2, kw, q0, k0, oi


def _att_specs():
    col = lambda off: pl.BlockSpec((T, HC), lambda g: (0, g + off * ATT_GROUPS))
    tables = [pl.BlockSpec((WIN_H, 1, HG, KWIN), lambda g: (0, g, 0, 0)),
              pl.BlockSpec((GRID_W, KWIN), lambda g: (0, 0)),
              pl.BlockSpec((HQ, HC), lambda g: (0, 0))]
    return col, tables, pltpu.VMEM((WIN_H, HQ, KWIN), F32)


def _att_fwd(qkv, bias_rows):
    valid_np, same_head_np = _att_tables()

    def body(q_ref, k_ref, v_ref, rows_ref, valid_ref, head_ref, o_ref, bias_s):
        same_head = head_ref[...] > 0.5
        _bias_tiles(rows_ref, valid_ref[...] > 0.5, bias_s)

        def row(r, carry):
            e, rl, _, _, q0, k0, _ = _att_scores(q_ref, k_ref, bias_s, same_head, r)
            o2 = _dot((e * rl).astype(BF16), v_ref[pl.ds(k0, KWIN), :])
            o_ref[pl.ds(q0, GRID_W), :] = _own_heads(o2).astype(BF16)
            return carry

        lax.fori_loop(0, ROWS, row, 0, unroll=ATT_UNROLL)

    col, tables, tiles = _att_specs()
    return pl.pallas_call(
        body, name="att_fwd",
        out_shape=jax.ShapeDtypeStruct((T, D_ATT), BF16),
        grid=(ATT_GROUPS,),
        in_specs=[col(0), col(1), col(2)] + tables,
        out_specs=col(0),
        scratch_shapes=[tiles],
        compiler_params=_params(dimension_semantics=("parallel",)),
    )(qkv, qkv, qkv, bias_rows, jnp.asarray(valid_np), jnp.asarray(same_head_np))


def _att_bwd(qkv, bias_rows, datt, after):
    valid_np, same_head_np = _att_tables()

    def body(q_ref, k_ref, v_ref, do_ref, rows_ref, valid_ref, head_ref, flip_ref,
             dqkv_ref, grows_ref, dk_acc, dv_acc, bias_s, gb_s):
        same_head = head_ref[...] > 0.5
        dk_acc[...] = jnp.zeros_like(dk_acc)
        dv_acc[...] = jnp.zeros_like(dv_acc)
        gb_s[...] = jnp.zeros_like(gb_s)
        _bias_tiles(rows_ref, valid_ref[...] > 0.5, bias_s)

        def row(r, carry):
            e, rl, q2, kw, q0, k0, oi = _att_scores(q_ref, k_ref, bias_s, same_head, r)
            do2 = _stacked(do_ref[pl.ds(q0, GRID_W), :], same_head)
            vw = v_ref[pl.ds(k0, KWIN), :]
            p = e * rl
            dp = _dot_nt(do2, vw)
            ds = p * (dp - jnp.sum(dp * p, axis=-1, keepdims=True))
            p16 = p.astype(BF16)
            ds16 = ds.astype(BF16)
            dv_acc[pl.ds(k0, KWIN), :] += _dot_tn(p16, do2)
            dk_acc[pl.ds(k0, KWIN), :] += _dot_tn(ds16, q2)
            dq2 = _dot(ds16, kw) * (DH ** -0.5)
            dqkv_ref[0, pl.ds(q0, GRID_W), :] = _own_heads(dq2).astype(BF16)
            gb_s[oi] += ds
            return carry

        lax.fori_loop(0, ROWS, row, 0, unroll=ATT_UNROLL)
        dqkv_ref[1] = dk_acc[...].astype(BF16)
        dqkv_ref[2] = dv_acc[...].astype(BF16)
        _bias_tile_grads(gb_s, flip_ref[...], grows_ref)

    col, tables, tiles = _att_specs()
    return pl.pallas_call(
        body, name="att_bwd",
        out_shape=(jax.ShapeDtypeStruct((3, T, D_ATT), BF16),
                   jax.ShapeDtypeStruct((ATT_GROUPS, WIN_H, HG, KWIN), F32)),
        grid=(ATT_GROUPS,),
        in_specs=[col(0), col(1), col(2), col(0)] + tables + [pl.BlockSpec((GRID_W, GRID_W), lambda g: (0, 0))],
        out_specs=(pl.BlockSpec((3, T, HC), lambda g: (0, 0, g)),
                   pl.BlockSpec((1, WIN_H, HG, KWIN), lambda g: (g, 0, 0, 0))),
        scratch_shapes=[pltpu.VMEM((T, HC), F32), pltpu.VMEM((T, HC), F32), tiles, tiles],
        compiler_params=_params(dimension_semantics=("parallel",)),
    )(qkv, qkv, qkv, datt, bias_rows, jnp.asarray(valid_np) + after, jnp.asarray(same_head_np),
      jnp.asarray(np.eye(GRID_W, dtype=np.float32)[::-1].copy()))


def _conv_taps(up):
    return (_shift_rows(up, 2), _shift_rows(up, 1), up, _shift_rows(up, -1))


def _pair_block_diag(w_pair, dup, same_half):
    return jnp.where(same_half, _dot(w_pair.astype(BF16), dup), 0.0).astype(BF16)


def _gates(u, u16, wa, ba, wi, bi, lam):
    r = _sigmoid(_dot(u16, wa) + ba)
    ig = _sigmoid(_dot(u16, wi) + bi)
    sp = _softplus(-lam)
    log_a = (-LRU_C) * r * sp
    a = jnp.exp(log_a)
    mult2 = jnp.maximum(_one_minus_square(log_a, a), 0.0)
    return r, ig, sp, a, jnp.sqrt(mult2), mult2


SCAN_BLOCKS = 8


def _scans(jobs):
    c = jobs[0][0].shape[1]
    nblk = T // 8
    rows = lax.broadcasted_iota(jnp.int32, (8, c), 0)

    def block(a, b, reverse):
        for s in (1, 2, 4):
            if reverse:
                keep = rows < 8 - s
                a_s = jnp.where(keep, pltpu.roll(a, 8 - s, 0), 1.0)
                b_s = jnp.where(keep, pltpu.roll(b, 8 - s, 0), 0.0)
            else:
                keep = rows >= s
                a_s = jnp.where(keep, pltpu.roll(a, s, 0), 1.0)
                b_s = jnp.where(keep, pltpu.roll(b, s, 0), 0.0)
            b = a * b_s + b
            a = a * a_s
        return a, b

    def step(i, carry):
        out = []
        for (a_ref, b_ref, h_ref, reverse), h_prev in zip(jobs, carry):
            for u in range(SCAN_BLOCKS):
                blk = i * SCAN_BLOCKS + u
                if reverse:
                    blk = nblk - 1 - blk
                t0 = pl.multiple_of(blk * 8, 8)
                a, b = block(a_ref[pl.ds(t0, 8), :], b_ref[pl.ds(t0, 8), :], reverse)
                h = a * h_prev + b
                h_ref[pl.ds(t0, 8), :] = h
                h_prev = jnp.broadcast_to(h[0:1] if reverse else h[7:8], (8, c))
            out.append(h_prev)
        return tuple(out)

    lax.fori_loop(0, nblk // SCAN_BLOCKS, step, tuple(jnp.zeros((8, c), F32) for _ in jobs))


def _rec_specs():
    tok = lambda off: pl.BlockSpec((T, CG), lambda g: (0, g + off))
    per_ch = lambda rows: pl.BlockSpec((rows, CG), lambda g: (0, g))
    wspec = pl.BlockSpec((2, 1, CG, REC_BLOCK), lambda g: (0, g, 0, 0))
    const = lambda shape: pl.BlockSpec(shape, lambda g: (0, 0))
    return tok, per_ch, wspec, const


def _rec_fwd(uy, conv_w, conv_b, w_a, b_a, w_i, b_i, lam):
    tok, per_ch, wspec, const = _rec_specs()

    def body(up_ref, yb_ref, cw_ref, cb_ref, wa_ref, ba_ref, wi_ref, bi_ref, lam_ref, dup_ref, half_ref,
             hf_ref, hb_ref, yrec_ref, am_ref, bx_f, bx_b):
        dup = dup_ref[...]
        same_half = half_ref[...] > 0.5
        taps = _conv_taps(up_ref[...])
        u = cb_ref[...]
        for j in range(4):
            u = u + taps[j] * cw_ref[j:j + 1, :]
        u16 = u.astype(BF16)
        for d, bx_s in enumerate((bx_f, bx_b)):
            wa = _pair_block_diag(wa_ref[d, 0], dup, same_half)
            wi = _pair_block_diag(wi_ref[d, 0], dup, same_half)
            _, ig, _, a, mult, _ = _gates(u, u16, wa, ba_ref[d:d + 1, :], wi, bi_ref[d:d + 1, :],
                                       lam_ref[d:d + 1, :])
            am_ref[2 * d] = a
            am_ref[2 * d + 1] = mult
            bx_s[...] = mult * (ig * u)
        _scans([(am_ref.at[0], bx_f, hf_ref, False), (am_ref.at[2], bx_b, hb_ref, True)])
        gelu, _ = _gelu_and_grad(yb_ref[...])
        yrec_ref[...] = ((hf_ref[...] + hb_ref[...]) * gelu).astype(BF16)

    return pl.pallas_call(
        body, name="rec_fwd",
        out_shape=(jax.ShapeDtypeStruct((T, D_REC), F32), jax.ShapeDtypeStruct((T, D_REC), F32),
                   jax.ShapeDtypeStruct((T, D_REC), BF16), jax.ShapeDtypeStruct((4, T, D_REC), F32)),
        grid=(N_CG,),
        in_specs=[tok(0), tok(N_CG), per_ch(4), per_ch(1), wspec, per_ch(2), wspec, per_ch(2), per_ch(2),
                  const((REC_BLOCK, CG)), const((CG, CG))],
        out_specs=(tok(0), tok(0), tok(0), pl.BlockSpec((4, T, CG), lambda g: (0, 0, g))),
        scratch_shapes=[pltpu.VMEM((T, CG), F32)] * 2,
        compiler_params=_params(dimension_semantics=("parallel",)),
    )(uy, uy, conv_w, conv_b, w_a, b_a, w_i, b_i, lam,
      jnp.asarray(_dup_table(), BF16), jnp.asarray(_pair_mask()))


def _rec_bwd(uy, hf, hb, am, dyrec, conv_w, conv_b, w_a, b_a, w_i, b_i, lam):
    tok, per_ch, wspec, const = _rec_specs()

    def body(up_ref, yb_ref, hf_ref, hb_ref, am_ref, dy_ref, cw_ref, cb_ref, wa_ref, ba_ref, wi_ref, bi_ref,
             lam_ref, dup_ref, dupt_ref, half_ref,
             duy_ref, dcw_ref, dcb_ref, dwa_ref, dba_ref, dwi_ref, dbi_ref, dlam_ref,
             a_s0, a_s1, dh_s, g_s0, g_s1):
        dup = dup_ref[...]
        dup_t = dupt_ref[...]
        same_half = half_ref[...] > 0.5
        taps = _conv_taps(up_ref[...])
        u = cb_ref[...]
        for j in range(4):
            u = u + taps[j] * cw_ref[j:j + 1, :]
        u16 = u.astype(BF16)
        gelu, dgelu = _gelu_and_grad(yb_ref[...])
        dy = dy_ref[...]
        duy_ref[1] = (dy * (hf_ref[...] + hb_ref[...]) * dgelu).astype(BF16)
        dh_s[...] = dy * gelu
        a_s0[...] = _shift_rows(am_ref[0], -1)
        a_s1[...] = _shift_rows(am_ref[2], 1)
        _scans([(a_s0, dh_s, g_s0, True), (a_s1, dh_s, g_s1, False)])
        du = jnp.zeros((T, CG), F32)
        for d, g_s in enumerate((g_s0, g_s1)):
            reverse = d == 1
            wa = _pair_block_diag(wa_ref[d, 0], dup, same_half)
            wi = _pair_block_diag(wi_ref[d, 0], dup, same_half)
            lam_d = lam_ref[d:d + 1, :]
            r = _sigmoid(_dot(u16, wa) + ba_ref[d:d + 1, :])
            ig = _sigmoid(_dot(u16, wi) + bi_ref[d:d + 1, :])
            sp = _softplus(-lam_d)
            a, mult = am_ref[2 * d], am_ref[2 * d + 1]
            mult2 = mult * mult
            g = g_s[...]
            h_prev = _shift_rows(hb_ref[...], -1) if reverse else _shift_rows(hf_ref[...], 1)
            da = g * h_prev
            dmult = g * (ig * u)
            dig = g * mult * u
            du = du + g * mult * ig
            dmult_dlog = jnp.where(mult2 > 0.0, -(a * a) * lax.rsqrt(mult2), 0.0)
            dlog_a = da * a + dmult * dmult_dlog
            dr = dlog_a * ((-LRU_C) * sp)
            dsp = jnp.sum(dlog_a * ((-LRU_C) * r), axis=0, keepdims=True)
            dlam_ref[d:d + 1, :] = dsp * (-_sigmoid(-lam_d))
            dga = dr * r * (1.0 - r)
            dgi = dig * ig * (1.0 - ig)
            dga16 = dga.astype(BF16)
            dgi16 = dgi.astype(BF16)
            du = du + _dot_nt(dga16, wa) + _dot_nt(dgi16, wi)
            dwa_ref[d, 0] = _dot_exact(jnp.where(same_half, _dot_tn(u16, dga16), 0.0), dup_t)
            dwi_ref[d, 0] = _dot_exact(jnp.where(same_half, _dot_tn(u16, dgi16), 0.0), dup_t)
            dba_ref[d:d + 1, :] = jnp.sum(dga, axis=0, keepdims=True)
            dbi_ref[d:d + 1, :] = jnp.sum(dgi, axis=0, keepdims=True)
        dcb_ref[...] = jnp.sum(du, axis=0, keepdims=True)
        for j in range(4):
            dcw_ref[j:j + 1, :] = jnp.sum(du * taps[j], axis=0, keepdims=True)
        dup_in = (_shift_rows(du, -2) * cw_ref[0:1, :] + _shift_rows(du, -1) * cw_ref[1:2, :]
                  + du * cw_ref[2:3, :] + _shift_rows(du, 1) * cw_ref[3:4, :])
        duy_ref[0] = dup_in.astype(BF16)

    wshape = jax.ShapeDtypeStruct((2, N_CG, CG, REC_BLOCK), F32)
    vec = lambda rows: jax.ShapeDtypeStruct((rows, D_REC), F32)
    dup_np = _dup_table()
    return pl.pallas_call(
        body, name="rec_bwd",
        out_shape=(jax.ShapeDtypeStruct((2, T, D_REC), BF16),
                   vec(4), vec(1), wshape, vec(2), wshape, vec(2), vec(2)),
        grid=(N_CG,),
        in_specs=[tok(0), tok(N_CG), tok(0), tok(0), pl.BlockSpec((4, T, CG), lambda g: (0, 0, g)), tok(0),
                  per_ch(4), per_ch(1), wspec, per_ch(2), wspec, per_ch(2), per_ch(2),
                  const((REC_BLOCK, CG)), const((CG, REC_BLOCK)), const((CG, CG))],
        out_specs=(pl.BlockSpec((2, T, CG), lambda g: (0, 0, g)),
                   per_ch(4), per_ch(1), wspec, per_ch(2), wspec, per_ch(2), per_ch(2)),
        scratch_shapes=[pltpu.VMEM((T, CG), F32)] * 5,
        compiler_params=_params(dimension_semantics=("parallel",)),
    )(uy, uy, hf, hb, am, dyrec, conv_w, conv_b, w_a, b_a, w_i, b_i, lam,
      jnp.asarray(dup_np, BF16), jnp.asarray(dup_np.T.copy()), jnp.asarray(_pair_mask()))


TM_MIX = 256


def _mix_specs():
    tok = lambda width, blk=0: pl.BlockSpec((TM_MIX, width), lambda i: (i, blk))
    full = lambda shape: pl.BlockSpec(shape, lambda i: (0, 0))
    return tok, full


def _mix_fwd(x, att, yrec, gg, w_att_o_t, w_rec_o, w_out):
    tok, full = _mix_specs()

    def body(x_ref, att_ref, yr_ref, ga_ref, gr_ref, wao_ref, wro_ref, wo_ref, x1_ref, mixed_ref):
        y_att = _dot_nt(att_ref[...], wao_ref[...])
        y_rec = _dot(yr_ref[...], wro_ref[...])
        mixed = (_sigmoid(ga_ref[...]) * y_att + _sigmoid(gr_ref[...]) * y_rec).astype(BF16)
        mixed_ref[...] = mixed
        x1_ref[...] = x_ref[...] + _dot(mixed, wo_ref[...])

    return pl.pallas_call(
        body, name="mix_fwd",
        out_shape=(jax.ShapeDtypeStruct((T, D), F32), jax.ShapeDtypeStruct((T, D), BF16)),
        grid=(T // TM_MIX,),
        in_specs=[tok(D), tok(D_ATT), tok(D_REC), tok(D, 0), tok(D, 1),
                  full((D, D_ATT)), full((D_REC, D)), full((D, D))],
        out_specs=(tok(D), tok(D)),
        compiler_params=_params(dimension_semantics=("parallel",)),
    )(x, att, yrec, gg, gg, w_att_o_t, w_rec_o, w_out)


def _mix_bwd(dx1, att, yrec, gg, w_att_o_t, w_rec_o, w_out, after):
    tok, full = _mix_specs()

    def body(dx_ref, att_ref, yr_ref, ga_ref, gr_ref, wao_ref, wro_ref, wo_ref, after_ref,
             dgg_ref, dya_ref, dyr_ref, datt_ref, dyrp_ref):
        dmixed = _dot_nt(dx_ref[...].astype(BF16), wo_ref[...])
        y_att = _dot_nt(att_ref[...], wao_ref[...])
        y_rec = _dot(yr_ref[...], wro_ref[...])
        sa = _sigmoid(ga_ref[...])
        sr = _sigmoid(gr_ref[...])
        dgg_ref[0] = (dmixed * y_att * sa * (1.0 - sa)).astype(BF16)
        dgg_ref[1] = (dmixed * y_rec * sr * (1.0 - sr)).astype(BF16)
        dya = (dmixed * sa).astype(BF16)
        dyr = (dmixed * sr).astype(BF16)
        dya_ref[...] = dya
        dyr_ref[...] = dyr
        datt_ref[...] = _dot(dya, wao_ref[...]).astype(BF16)
        dyrp_ref[...] = _dot_nt(dyr, wro_ref[...])

    return pl.pallas_call(
        body, name="mix_bwd",
        out_shape=(jax.ShapeDtypeStruct((2, T, D), BF16),
                   jax.ShapeDtypeStruct((T, D), BF16), jax.ShapeDtypeStruct((T, D), BF16),
                   jax.ShapeDtypeStruct((T, D_ATT), BF16), jax.ShapeDtypeStruct((T, D_REC), F32)),
        grid=(T // TM_MIX,),
        in_specs=[tok(D), tok(D_ATT), tok(D_REC), tok(D, 0), tok(D, 1),
                  full((D, D_ATT)), full((D_REC, D)), full((D, D)), pl.BlockSpec(memory_space=pl.ANY)],
        out_specs=(pl.BlockSpec((2, TM_MIX, D), lambda i: (0, i, 0)),
                   tok(D), tok(D), tok(D_ATT), tok(D_REC)),
        compiler_params=_params(dimension_semantics=("parallel",)),
    )(dx1, att, yrec, gg, gg, w_att_o_t, w_rec_o, w_out, after)


TM_FFN = 256
FF_CHUNK = 1024


def _ffn_loss(x1, target, g2, gf, w_ff1_t, w_ff2):
    n_chunks = D_FF // FF_CHUNK

    def body(x1_ref, tg_ref, g2_ref, gf_ref, w1_hbm, w2_hbm,
             loss_ref, dx1_ref, h2_ref, act_ref, dpre_ref, dx2_ref, dg2_ref, dgf_ref,
             w1, w2, relu_s):
        i = pl.program_id(0)

        @pl.when(i == 0)
        def _():
            pltpu.sync_copy(w1_hbm, w1)
            pltpu.sync_copy(w2_hbm, w2)
            loss_ref[...] = jnp.zeros_like(loss_ref)
            dg2_ref[...] = jnp.zeros_like(dg2_ref)
            dgf_ref[...] = jnp.zeros_like(dgf_ref)

        x1v = x1_ref[...]
        r2 = lax.rsqrt(jnp.mean(x1v * x1v, axis=-1, keepdims=True) + EPS)
        xh2 = x1v * r2
        h2 = (xh2 * g2_ref[...]).astype(BF16)
        h2_ref[...] = h2
        x2 = x1v
        for c in range(n_chunks):
            ff = slice(c * FF_CHUNK, (c + 1) * FF_CHUNK)
            rl = jnp.maximum(_dot_nt(h2, w1[ff, :]), 0.0)
            relu_s[:, ff] = rl
            act = (rl * rl).astype(BF16)
            act_ref[:, ff] = act
            x2 = x2 + _dot(act, w2[ff, :])
        r3 = lax.rsqrt(jnp.mean(x2 * x2, axis=-1, keepdims=True) + EPS)
        xh3 = x2 * r3
        err = xh3 * gf_ref[...] - tg_ref[...]
        loss_ref[...] += 0.5 * jnp.sum(jnp.mean(err * err, axis=-1, keepdims=True))
        dy = err * (1.0 / D)
        dgf_ref[...] += jnp.sum(dy * xh3, axis=0, keepdims=True)
        dx2 = _rms_bwd(dy, xh3, r3, gf_ref[...])
        dx2_16 = dx2.astype(BF16)
        dx2_ref[...] = dx2_16
        dh2 = jnp.zeros((TM_FFN, D), F32)
        for c in range(n_chunks):
            ff = slice(c * FF_CHUNK, (c + 1) * FF_CHUNK)
            dpre = (_dot_nt(dx2_16, w2[ff, :]) * (2.0 * relu_s[:, ff])).astype(BF16)
            dpre_ref[:, ff] = dpre
            dh2 = dh2 + _dot(dpre, w1[ff, :])
        dg2_ref[...] += jnp.sum(dh2 * xh2, axis=0, keepdims=True)
        dx1_ref[...] = dx2 + _rms_bwd(dh2, xh2, r2, g2_ref[...])

    tok = lambda width: pl.BlockSpec((TM_FFN, width), lambda i: (i, 0))
    vec = pl.BlockSpec((1, D), lambda i: (0, 0))
    hbm = pl.BlockSpec(memory_space=pl.ANY)
    return pl.pallas_call(
        body, name="ffn_loss",
        out_shape=(jax.ShapeDtypeStruct((8, 128), F32), jax.ShapeDtypeStruct((T, D), F32),
                   jax.ShapeDtypeStruct((T, D), BF16), jax.ShapeDtypeStruct((T, D_FF), BF16),
                   jax.ShapeDtypeStruct((T, D_FF), BF16), jax.ShapeDtypeStruct((T, D), BF16),
                   jax.ShapeDtypeStruct((1, D), F32), jax.ShapeDtypeStruct((1, D), F32)),
        grid=(T // TM_FFN,),
        in_specs=[tok(D), tok(D), vec, vec, hbm, hbm],
        out_specs=(pl.BlockSpec((8, 128), lambda i: (0, 0)), tok(D), tok(D), tok(D_FF), tok(D_FF), tok(D),
                   vec, vec),
        scratch_shapes=[pltpu.VMEM((D_FF, D), BF16), pltpu.VMEM((D_FF, D), BF16),
                        pltpu.VMEM((TM_FFN, D_FF), F32)],
        compiler_params=_params(dimension_semantics=("arbitrary",)),
    )(x1, target, g2, gf, w_ff1_t, w_ff2)


def _local_step(x, target, p, late_weights, reduce_first, reduce_early):
    bias = _rpb_rows(p["rpb"])
    pairs = lambda w: w.reshape(2, N_CG, CG, REC_BLOCK)
    w_a, w_i = pairs(p["w_rg_a"]), pairs(p["w_rg_i"])
    rec_params = (p["conv_w"], p["conv_b"], w_a, p["b_rg_a"], w_i, p["b_rg_i"], p["lru_lambda"])

    qkv, uy, gg, h = _in_proj(x, p["ln1_g"], p["w_in_t"], p["b_in"])
    att = _att_fwd(qkv, bias)
    hf, hb, yrec, am = _rec_fwd(uy, *rec_params)
    p = {**p, **late_weights(yrec, 0)}
    x1, mixed = _mix_fwd(x, att, yrec, gg, p["w_att_o_t"], p["w_rec_o"], p["w_out"])
    p = {**p, **late_weights(x1, 1)}
    loss8, dx1, h2, act, dpre, dx2, g_ln2, g_lnf = _ffn_loss(
        x1, target, p["ln2_g"], p["lnf_g"], p["w_ff1_t"], p["w_ff2"])

    grads = {"ln2_g": g_ln2, "lnf_g": g_lnf,
             "w_ff1_t": _matmul(dpre, h2, "tn", BF16, "g_w_ff1"),
             "w_ff2": _matmul(act, dx2, "tn", BF16, "g_w_ff2")}
    dgg, dya, dyr, datt, dyrp = _mix_bwd(dx1, att, yrec, gg, p["w_att_o_t"], p["w_rec_o"], p["w_out"],
                                         reduce_first(grads, None))
    lam_after = rec_params[-1] + reduce_first(None, dgg)[0, 0]
    duy, g_cw, g_cb, g_wa, g_ba, g_wi, g_bi, g_lam = _rec_bwd(uy, hf, hb, am, dyrp, *rec_params[:-1], lam_after)
    blocks = lambda g: g.reshape(2, N_REC_BLOCKS, REC_BLOCK, REC_BLOCK)
    grads.update({
        "w_att_o_t": _matmul(dya, att, "tn", BF16, "g_w_att_o"),
        "conv_w": g_cw, "conv_b": g_cb, "w_rg_a": blocks(g_wa), "b_rg_a": g_ba,
        "w_rg_i": blocks(g_wi), "b_rg_i": g_bi, "lru_lambda": g_lam,
        "w_rec_o": _matmul(yrec, dyr, "tn", BF16, "g_w_rec_o"),
        "w_out": _matmul(mixed, dx1, "tn", BF16, "g_w_out"),
    })
    dqkv, gbias = _att_bwd(qkv, bias, datt, reduce_early(grads))
    dz = (dqkv, duy, dgg)
    grad_x, g_ln1 = _dh_norm1_bwd(dz, p["w_in_t"], x, p["ln1_g"], dx1)
    g_w_in_t, g_b_in = _grad_w_in(dz, h)
    grads.update(ln1_g=g_ln1, w_in_t=g_w_in_t, b_in=g_b_in, rpb=_rpb_fold(gbias))
    return loss8[0:1, 0:1], grad_x, grads


MESH_ID = pl.DeviceIdType.MESH
ANY = pl.BlockSpec(memory_space=pl.ANY)

CHAN_BLOCK_ROWS = 32
GATE_ROWS = 2 * 2 * N_REC_BLOCKS * REC_BLOCK * REC_BLOCK // (N_DEV * D)
SECTIONS = (("w_in_t", 704, D), ("w_rec_o", 128, D), ("w_out", 128, D), ("w_ff1_t", 512, D),
            ("w_ff2", 512, D), ("chan", CHAN_BLOCK_ROWS, D), ("w_att_o_t", 128, D_ATT),
            ("gates", GATE_ROWS, D))
N_SEC = len(SECTIONS)
N_CHAN_ROWS = 10
CHAN = (("conv_w", 4), ("b_rg_a", 2), ("b_rg_i", 2), ("lru_lambda", 2))


def _position():
    return lax.axis_index("x"), lax.axis_index("y"), lax.axis_index("c")


def _other_chips(x, y):
    return [(1 - x, y), (x, 1 - y), (1 - x, 1 - y)]


PASS_ON_IDS, PAIR_EARLY_ID, PAIR_LATE_ID, PAIR_FIRST_ID = (1, 4), 2, 3, 5


def _pair_handshake(x, y, c):
    barrier = pltpu.get_barrier_semaphore()
    pl.semaphore_signal(barrier, inc=1, device_id=(x, y, 1 - c), device_id_type=MESH_ID)
    pl.semaphore_wait(barrier, 1)


def _block_of(ref, dev, rows):
    return ref.at[pl.ds(pl.multiple_of(dev * rows, 16), rows)]


def _all_gather(shards, name):
    ns = len(shards)

    def body(*refs):
        x_refs, out_refs, done_ref = refs[:ns], refs[ns:2 * ns], refs[2 * ns]
        send_sems, recv_sems, local_sems = refs[2 * ns + 1:]
        done_ref[0, 0] = 0.0
        x, y, c = _position()
        me, sibling = (x, y, c), (x, y, 1 - c)
        x_nbr, y_nbr, diagonal = _other_chips(x, y)
        north = c == 1
        relay_from = (jnp.where(north, x_nbr[0], y_nbr[0]), jnp.where(north, x_nbr[1], y_nbr[1]))
        relay_to = (jnp.where(north, y_nbr[0], x_nbr[0]), jnp.where(north, y_nbr[1], x_nbr[1]))

        def rows(s, px, py, pc):
            return _block_of(out_refs[s], 4 * px + 2 * py + pc, shards[s].shape[0])

        def copy(k, s, block, to, from_shard=False):
            return pltpu.make_async_remote_copy(
                src_ref=x_refs[s] if from_shard else rows(s, *block), dst_ref=rows(s, *block),
                send_sem=send_sems.at[k * ns + s], recv_sem=recv_sems.at[k * ns + s],
                device_id=to, device_id_type=MESH_ID)

        sections = range(ns)
        mine = [pltpu.make_async_copy(x_refs[s], rows(s, *me), local_sems.at[s]) for s in sections]
        sent = [copy(k, s, me, to, True) for k, to in enumerate((sibling, (*x_nbr, c), (*y_nbr, c)))
                for s in sections]
        for cp in mine + sent:
            cp.start()
        for s in sections:
            copy(1, s, (*x_nbr, c), me).wait_recv()
            copy(2, s, (*y_nbr, c), me).wait_recv()
            sent += [copy(3, s, (*relay_from, c), (*relay_to, c)),
                     copy(4, s, (*x_nbr, c), sibling), copy(5, s, (*y_nbr, c), sibling)]
            for cp in sent[-3:]:
                cp.start()
        for s in sections:
            copy(3, s, (*diagonal, c), me).wait_recv()
            sent.append(copy(6, s, (*diagonal, c), sibling))
            sent[-1].start()
        for s in sections:
            copy(0, s, sibling, me).wait_recv()
            for k, chip in ((4, x_nbr), (5, y_nbr), (6, diagonal)):
                copy(k, s, (*chip, 1 - c), me).wait_recv()
        for cp in sent:
            cp.wait_send()
        for cp in mine:
            cp.wait()

    return pl.pallas_call(
        body, name=name,
        out_shape=tuple(jax.ShapeDtypeStruct((N_DEV * s.shape[0], s.shape[1]), s.dtype) for s in shards)
        + (jax.ShapeDtypeStruct((1, 1), F32),),
        in_specs=[ANY] * ns,
        out_specs=(ANY,) * ns + (pl.BlockSpec(memory_space=pltpu.SMEM),),
        scratch_shapes=[pltpu.SemaphoreType.DMA((7 * ns,)), pltpu.SemaphoreType.DMA((7 * ns,)),
                        pltpu.SemaphoreType.DMA((ns,))],
    )(*shards)


HBM = pl.BlockSpec(memory_space=pltpu.HBM)
SEM = pl.BlockSpec(memory_space=pltpu.SEMAPHORE)
EFFECT = pltpu.SideEffectType.DATAFLOW_SIDE_EFFECTING


def _in_hbm(a):
    return pltpu.with_memory_space_constraint(a, pltpu.HBM)


def _first_hop_copies(shards, x_refs, zones, send_sems, recv_sems):
    ns = len(shards)
    x, y, c = _position()
    targets = [(x, y, 1 - c)] + [(cx, cy, c) for cx, cy in _other_chips(x, y)]
    return [pltpu.make_async_remote_copy(
        src_ref=x_refs[s], dst_ref=_block_of(zones[s], 4 * x + 2 * y + c, shards[s].shape[0]),
        send_sem=send_sems.at[k * ns + s], recv_sem=recv_sems.at[k * ns + s],
        device_id=to, device_id_type=MESH_ID)
        for k, to in enumerate(targets) for s in range(ns)]


def _after_all(arrays, name):
    def body(*refs):
        refs[-1][...] = jnp.zeros_like(refs[-1])

    return pl.pallas_call(
        body, name=name,
        out_shape=jax.ShapeDtypeStruct((8, LANES), F32),
        in_specs=[pl.BlockSpec(memory_space=pl.ANY)] * len(arrays),
        out_specs=pl.BlockSpec(memory_space=pltpu.VMEM),
    )(*arrays)


def _own_blocks_placed(shards, after):
    ns = len(shards)
    x, y, c = _position()
    me = jnp.reshape(4 * x + 2 * y + c, (1,)).astype(jnp.int32)
    shards = [*shards[:-1], shards[-1] + after.astype(shards[-1].dtype)]

    def body(me_ref, *refs):
        for s in range(ns):
            refs[ns + s][...] = refs[s][...]

    return pl.pallas_call(
        body, name="own_blocks_placed",
        out_shape=tuple(jax.ShapeDtypeStruct((N_DEV * s.shape[0], s.shape[1]), s.dtype) for s in shards),
        grid_spec=pltpu.PrefetchScalarGridSpec(
            num_scalar_prefetch=1, grid=(1,),
            in_specs=[pl.BlockSpec(s.shape, lambda i, me: (0, 0)) for s in shards],
            out_specs=tuple(pl.BlockSpec(s.shape, lambda i, me: (me[0], 0)) for s in shards)),
        compiler_params=_params(dimension_semantics=("arbitrary",)),
    )(me, *shards)


def _gather_start(shards, after, name):
    ns = len(shards)
    zones = _own_blocks_placed(shards, after)

    def body(*refs):
        for cp in _first_hop_copies(shards, refs[:ns], refs[ns:2 * ns], refs[2 * ns], refs[2 * ns + 1]):
            cp.start()
        refs[-1][...] = jnp.zeros_like(refs[-1])

    out = pl.pallas_call(
        body, name=name,
        out_shape=(pltpu.SemaphoreType.DMA((4 * ns,)), pltpu.SemaphoreType.DMA((4 * ns,)),
                   *[pltpu.HBM(a.shape, a.dtype) for a in (*shards, *zones)],
                   jax.ShapeDtypeStruct((8, LANES), F32)),
        in_specs=[HBM] * (2 * ns),
        out_specs=(SEM, SEM, *[HBM] * (2 * ns), pl.BlockSpec(memory_space=pltpu.VMEM)),
        input_output_aliases={i: 2 + i for i in range(2 * ns)},
        compiler_params=pltpu.CompilerParams(has_side_effects=EFFECT),
    )(*[_in_hbm(a) for a in shards], *[_in_hbm(a) for a in zones])
    return out[0], out[1], out[2:2 + ns], out[2 + ns:2 + 2 * ns], out[-1]


def _gather_wait(send_sems, recv_sems, shards, zones, which, after, name):
    ns = len(shards)

    def body(*refs):
        copies = _first_hop_copies(shards, refs[:ns], refs[ns:2 * ns], refs[2 * ns], refs[2 * ns + 1])
        for i, cp in enumerate(copies):
            if i % ns in which:
                cp.wait_send()
                cp.wait_recv()

    out = pl.pallas_call(
        body, name=name,
        out_shape=tuple(pltpu.HBM(a.shape, a.dtype) for a in (*shards, *zones)),
        in_specs=[HBM] * (2 * ns) + [SEM, SEM, ANY],
        out_specs=(HBM,) * (2 * ns),
        input_output_aliases={i: i for i in range(2 * ns)},
        compiler_params=pltpu.CompilerParams(has_side_effects=EFFECT),
    )(*shards, *zones, send_sems, recv_sems, after)
    return out[:ns], out[ns:]


def _gather_pass_on(rows, zones, barrier_id, name):
    ns = len(zones)

    def body(*refs):
        in_refs, out_refs = refs[:ns], refs[ns:2 * ns]
        send_sems, recv_sems = refs[2 * ns:]
        x, y, c = _position()
        _pair_handshake(x, y, c)
        copies = [pltpu.make_async_remote_copy(
            src_ref=_block_of(in_refs[s], 4 * cx + 2 * cy + c, rows[s]),
            dst_ref=_block_of(out_refs[s], 4 * cx + 2 * cy + c, rows[s]),
            send_sem=send_sems.at[j * ns + s], recv_sem=recv_sems.at[j * ns + s],
            device_id=(x, y, 1 - c), device_id_type=MESH_ID)
            for j, (cx, cy) in enumerate(_other_chips(x, y)) for s in range(ns)]
        for cp in copies:
            cp.start()
        for cp in copies:
            cp.wait_recv()
        for cp in copies:
            cp.wait_send()

    return pl.pallas_call(
        body, name=name,
        out_shape=tuple(jax.ShapeDtypeStruct(z.shape, z.dtype) for z in zones),
        in_specs=[ANY] * ns, out_specs=(ANY,) * ns,
        input_output_aliases={i: i for i in range(ns)},
        scratch_shapes=[pltpu.SemaphoreType.DMA((3 * ns,)), pltpu.SemaphoreType.DMA((3 * ns,))],
        compiler_params=pltpu.CompilerParams(collective_id=barrier_id),
    )(*zones)


def _pair_copies(sections, g_refs, land, send_sems, recv_sems):
    ns = len(sections)
    x, y, c = _position()
    return [pltpu.make_async_remote_copy(
        src_ref=_block_of(g_refs[s], 2 * k + 1 - c, rows), dst_ref=land[s].at[k],
        send_sem=send_sems.at[k * ns + s], recv_sem=recv_sems.at[k * ns + s],
        device_id=(x, y, 1 - c), device_id_type=MESH_ID)
        for k in range(N_CHIPS) for s, (_, rows, _) in enumerate(sections)]


def _pair_exchange_start(sections, grads, barrier_id, name):
    ns = len(sections)

    def body(*refs):
        _pair_handshake(*_position())
        for cp in _pair_copies(sections, refs[:ns], refs[ns:2 * ns], refs[2 * ns], refs[2 * ns + 1]):
            cp.start()
        refs[-1][...] = jnp.zeros_like(refs[-1])

    zones = [lax.empty((N_CHIPS, rows, cols), BF16) for _, rows, cols in sections]
    n = N_CHIPS * ns
    out = pl.pallas_call(
        body, name=name,
        out_shape=(pltpu.SemaphoreType.DMA((n,)), pltpu.SemaphoreType.DMA((n,)),
                   *[pltpu.HBM(a.shape, a.dtype) for a in (*grads, *zones)],
                   jax.ShapeDtypeStruct((8, LANES), F32)),
        in_specs=[HBM] * (2 * ns),
        out_specs=(SEM, SEM, *[HBM] * (2 * ns), pl.BlockSpec(memory_space=pltpu.VMEM)),
        input_output_aliases={i: 2 + i for i in range(2 * ns)},
        compiler_params=pltpu.CompilerParams(has_side_effects=EFFECT, collective_id=barrier_id),
    )(*[_in_hbm(a) for a in grads], *[_in_hbm(a) for a in zones])
    return out[0], out[1], out[2:2 + ns], out[2 + ns:2 + 2 * ns], out[-1]


def _pair_exchange_wait(sections, send_sems, recv_sems, grads, zones, after, name):
    ns = len(sections)

    def body(*refs):
        for cp in _pair_copies(sections, refs[:ns], refs[ns:2 * ns], refs[2 * ns], refs[2 * ns + 1]):
            cp.wait_send()
            cp.wait_recv()

    out = pl.pallas_call(
        body, name=name,
        out_shape=tuple(pltpu.HBM(a.shape, a.dtype) for a in (*grads, *zones)),
        in_specs=[HBM] * (2 * ns) + [SEM, SEM, ANY],
        out_specs=(HBM,) * (2 * ns),
        input_output_aliases={i: i for i in range(2 * ns)},
        compiler_params=pltpu.CompilerParams(has_side_effects=EFFECT),
    )(*grads, *zones, send_sems, recv_sems, after)
    return out[:ns], out[ns:]


def _pair_add(sections, grads, got, core, name):
    ns = len(sections)

    def body(core_ref, *refs):
        g_refs, got_refs, p_refs = refs[:ns], refs[ns:2 * ns], refs[2 * ns:]
        for s in range(ns):
            p_refs[s][0] = (g_refs[s][...].astype(F32) + got_refs[s][0].astype(F32)).astype(BF16)

    slot = [pl.BlockSpec((1, rows, cols), lambda k, c: (k, 0, 0)) for _, rows, cols in sections]
    return pl.pallas_call(
        body, name=name,
        out_shape=tuple(jax.ShapeDtypeStruct((N_CHIPS, rows, cols), BF16) for _, rows, cols in sections),
        grid_spec=pltpu.PrefetchScalarGridSpec(
            num_scalar_prefetch=1, grid=(N_CHIPS,),
            in_specs=[pl.BlockSpec((rows, cols), lambda k, c: (2 * k + c[0], 0)) for _, rows, cols in sections]
            + slot,
            out_specs=tuple(slot)),
        compiler_params=_params(dimension_semantics=("parallel",)),
    )(core, *grads, *got)


def _chip_copies(sections, p_refs, land, send_sems, recv_sems):
    ns = len(sections)
    x, y, c = _position()
    return [pltpu.make_async_remote_copy(
        src_ref=p_refs[s].at[2 * cx + cy], dst_ref=land[s].at[j],
        send_sem=send_sems.at[j * ns + s], recv_sem=recv_sems.at[j * ns + s],
        device_id=(cx, cy, c), device_id_type=MESH_ID)
        for j, (cx, cy) in enumerate(_other_chips(x, y)) for s in range(ns)]


def _chip_exchange(sections, parts, name):
    ns = len(sections)

    def body(*refs):
        copies = _chip_copies(sections, refs[:ns], refs[ns:2 * ns], *refs[2 * ns:])
        for cp in copies:
            cp.start()
        for cp in copies:
            cp.wait_recv()
        for cp in copies:
            cp.wait_send()

    n = 3 * ns
    return pl.pallas_call(
        body, name=name,
        out_shape=tuple(jax.ShapeDtypeStruct((3, rows, cols), BF16) for _, rows, cols in sections),
        in_specs=[ANY] * ns, out_specs=(ANY,) * ns,
        scratch_shapes=[pltpu.SemaphoreType.DMA((n,)), pltpu.SemaphoreType.DMA((n,))],
    )(*parts)


def _chip_exchange_start(sections, parts, name):
    ns = len(sections)

    def body(*refs):
        p_refs, land = refs[:ns], refs[ns:2 * ns]
        send_sems, recv_sems = refs[2 * ns], refs[2 * ns + 1]
        token = refs[-1]
        for cp in _chip_copies(sections, p_refs, land, send_sems, recv_sems):
            cp.start()
        token[...] = jnp.zeros_like(token)

    zones = [lax.empty((3, rows, cols), BF16) for _, rows, cols in sections]
    out = pl.pallas_call(
        body, name=name,
        out_shape=(pltpu.SemaphoreType.DMA((3 * ns,)), pltpu.SemaphoreType.DMA((3 * ns,)),
                   *[pltpu.HBM(a.shape, a.dtype) for a in parts], *[pltpu.HBM(a.shape, a.dtype) for a in zones],
                   jax.ShapeDtypeStruct((8, LANES), F32)),
        in_specs=[HBM] * (2 * ns),
        out_specs=(SEM, SEM, *[HBM] * (2 * ns), pl.BlockSpec(memory_space=pltpu.VMEM)),
        input_output_aliases={i: 2 + i for i in range(2 * ns)},
        compiler_params=pltpu.CompilerParams(has_side_effects=EFFECT),
    )(*[_in_hbm(a) for a in parts], *[_in_hbm(a) for a in zones])
    return out[0], out[1], out[2:2 + ns], out[2 + ns:2 + 2 * ns], out[-1]


def _chip_exchange_wait(sections, send_sems, recv_sems, parts, zones, after, name):
    ns = len(sections)

    def body(*refs):
        p_refs, land = refs[:ns], refs[ns:2 * ns]
        for cp in _chip_copies(sections, p_refs, land, refs[2 * ns], refs[2 * ns + 1]):
            cp.wait_send()
            cp.wait_recv()

    out = pl.pallas_call(
        body, name=name,
        out_shape=tuple(pltpu.HBM(a.shape, a.dtype) for a in (*parts, *zones)),
        in_specs=[HBM] * (2 * ns) + [SEM, SEM, ANY],
        out_specs=(HBM,) * (2 * ns),
        input_output_aliases={i: i for i in range(2 * ns)},
        compiler_params=pltpu.CompilerParams(has_side_effects=EFFECT),
    )(*parts, *zones, send_sems, recv_sems, after)
    return out[:ns], out[ns:]


def _grad_finish(sections, parts, far, chip, name):
    ns = len(sections)

    def body(chip_ref, *refs):
        p_refs, b_refs, g_refs = refs[:ns], refs[ns:2 * ns], refs[2 * ns:]
        for s in range(ns):
            g = p_refs[s][0].astype(F32)
            for j in range(3):
                g = g + b_refs[s][j].astype(F32)
            g_refs[s][...] = g

    half = [(rows // 2, cols) for _, rows, cols in sections]
    return pl.pallas_call(
        body, name=name,
        out_shape=tuple(jax.ShapeDtypeStruct((rows, cols), F32) for _, rows, cols in sections),
        grid_spec=pltpu.PrefetchScalarGridSpec(
            num_scalar_prefetch=1, grid=(2,),
            in_specs=[pl.BlockSpec((1, r, c), lambda i, chip: (chip[0], i, 0)) for r, c in half]
            + [pl.BlockSpec((3, r, c), lambda i, chip: (0, i, 0)) for r, c in half],
            out_specs=tuple(pl.BlockSpec((r, c), lambda i, chip: (i, 0)) for r, c in half)),
        compiler_params=_params(dimension_semantics=("parallel",)),
    )(chip, *parts, *far)


def _sum_devices(parts, rows, name):
    cols = parts.shape[1]
    tr = rows // 2

    def body(*refs):
        s = refs[0][...].astype(F32)
        for d in range(1, N_DEV):
            s = s + refs[d][...].astype(F32)
        refs[N_DEV][...] = s

    return pl.pallas_call(
        body, name=name,
        out_shape=jax.ShapeDtypeStruct((rows, cols), F32),
        grid=(2,),
        in_specs=[pl.BlockSpec((tr, cols), lambda i, d=d: (2 * d + i, 0)) for d in range(N_DEV)],
        out_specs=pl.BlockSpec((tr, cols), lambda i: (i, 0)),
        compiler_params=_params(dimension_semantics=("parallel",)),
    )(*([parts] * N_DEV))


def _adamw_step(w_ref, g_ref, m_ref, v_ref, d_ref, nm_ref, nv_ref):
    c1 = 1.0 / (1.0 - ADAM_B1 ** ADAM_STEP)
    c2 = 1.0 / (1.0 - ADAM_B2 ** ADAM_STEP)
    gv = g_ref[...]
    nm = ADAM_B1 * m_ref[...] + (1.0 - ADAM_B1) * gv
    nv = ADAM_B2 * v_ref[...] + (1.0 - ADAM_B2) * (gv * gv)
    nm_ref[...] = nm
    nv_ref[...] = nv
    d_ref[...] = (-ADAM_LR) * ((nm * c1) / (jnp.sqrt(nv * c2) + ADAM_EPS) + ADAM_WD * w_ref[...])


def _adamw_small(params, name):
    n = len(params)

    def body(*refs):
        for k in range(n):
            _adamw_step(*refs[4 * k:4 * k + 4], *refs[4 * n + 3 * k:4 * n + 3 * k + 3])

    out = pl.pallas_call(
        body, name=name,
        out_shape=tuple(jax.ShapeDtypeStruct(p[0].shape, F32) for p in params for _ in range(3)),
    )(*[a for p in params for a in p])
    return [out[3 * k:3 * k + 3] for k in range(n)]


def _adamw(w, g, m, v, name):
    rows, cols = w.shape
    tr = rows
    while tr * cols * 4 > (1 << 20) and tr % 16 == 0:
        tr //= 2

    def body(*refs):
        _adamw_step(*refs)

    spec = pl.BlockSpec((tr, cols), lambda i: (i, 0))
    shape = jax.ShapeDtypeStruct((rows, cols), F32)
    return pl.pallas_call(
        body, name=name,
        out_shape=(shape, shape, shape),
        grid=(rows // tr,),
        in_specs=[spec] * 4, out_specs=(spec,) * 3,
        compiler_params=_params(dimension_semantics=("parallel",)),
    )(w, g, m, v)


NAMES = ("ln1_g", "w_in", "b_in", "rpb", "w_att_o", "conv_w", "conv_b", "w_rg_a", "b_rg_a", "w_rg_i",
         "b_rg_i", "lru_lambda", "w_rec_o", "w_out", "ln2_g", "w_ff1", "w_ff2", "lnf_g")
TRANSPOSED = {"w_in": "w_in_t", "w_att_o": "w_att_o_t", "w_ff1": "w_ff1_t"}
ROW_SHARDED = ("w_rec_o", "w_out", "w_ff2")
REPLICATED = (("ln1_g", (1, D)), ("b_in", (1, D_IN)), ("rpb", (N_HEADS * N_RPB_R, N_RPB_C)),
              ("conv_b", (1, D_REC)), ("w_rg_a", (2 * N_REC_BLOCKS * REC_BLOCK, REC_BLOCK)),
              ("w_rg_i", (2 * N_REC_BLOCKS * REC_BLOCK, REC_BLOCK)), ("ln2_g", (1, D)), ("lnf_g", (1, D)))
GATE_BLOCKS = ("w_rg_a", "w_rg_i")
SMALL_ROWS = 112


def _chan_bits(vectors):
    chan = jnp.concatenate(vectors, axis=0)
    bits = lax.bitcast_convert_type(chan, BF16).reshape(-1)
    return jnp.pad(bits, (0, CHAN_BLOCK_ROWS * D - bits.shape[0])).reshape(CHAN_BLOCK_ROWS, D)


def _chan_from_bits(gathered):
    bits = gathered.reshape(N_DEV, CHAN_BLOCK_ROWS * D)[:, :2 * N_CHAN_ROWS * LANES]
    chan = lax.bitcast_convert_type(bits.reshape(N_DEV, N_CHAN_ROWS, LANES, 2), F32)
    return chan.transpose(1, 0, 2).reshape(N_CHAN_ROWS, D)


def kernel(x, ln1_g, w_in, b_in, rpb, w_att_o, conv_w, conv_b, w_rg_a, b_rg_a, w_rg_i, b_rg_i, lru_lambda, w_rec_o, w_out, ln2_g, w_ff1, w_ff2, lnf_g, loss_target, m_ln1_g, m_w_in, m_b_in, m_rpb, m_w_att_o, m_conv_w, m_conv_b, m_w_rg_a, m_b_rg_a, m_w_rg_i, m_b_rg_i, m_lru_lambda, m_w_rec_o, m_w_out, m_ln2_g, m_w_ff1, m_w_ff2, m_lnf_g, v_ln1_g, v_w_in, v_b_in, v_rpb, v_w_att_o, v_conv_w, v_conv_b, v_w_rg_a, v_b_rg_a, v_w_rg_i, v_b_rg_i, v_lru_lambda, v_w_rec_o, v_w_out, v_ln2_g, v_w_ff1, v_w_ff2, v_lnf_g):
    w = dict(zip(NAMES, (ln1_g, w_in, b_in, rpb, w_att_o, conv_w, conv_b, w_rg_a, b_rg_a, w_rg_i,
                         b_rg_i, lru_lambda, w_rec_o, w_out, ln2_g, w_ff1, w_ff2, lnf_g)))
    m = dict(zip(NAMES, (m_ln1_g, m_w_in, m_b_in, m_rpb, m_w_att_o, m_conv_w, m_conv_b, m_w_rg_a,
                         m_b_rg_a, m_w_rg_i, m_b_rg_i, m_lru_lambda, m_w_rec_o, m_w_out, m_ln2_g,
                         m_w_ff1, m_w_ff2, m_lnf_g)))
    v = dict(zip(NAMES, (v_ln1_g, v_w_in, v_b_in, v_rpb, v_w_att_o, v_conv_w, v_conv_b, v_w_rg_a,
                         v_b_rg_a, v_w_rg_i, v_b_rg_i, v_lru_lambda, v_w_rec_o, v_w_out, v_ln2_g,
                         v_w_ff1, v_w_ff2, v_lnf_g)))
    xi, yi, ci = _position()

    shard = {t: w[n][0].T.astype(BF16) for n, t in TRANSPOSED.items()}
    shard.update({n: w[n][0].astype(BF16) for n in ROW_SHARDED})
    shard["chan"] = _chan_bits([w[n][0] for n, _ in CHAN])
    first, later = ("w_in_t", "chan"), ("w_rec_o", "w_out", "w_att_o_t", "w_ff1_t", "w_ff2")
    *gathered, done = _all_gather([shard[n] for n in first], "weight_all_gather")
    p = dict(zip(first, gathered))
    send_sems, recv_sems, sent, zones, token = _gather_start([shard[n] for n in later], done,
                                                             "weight_gather_start")

    travelling = {"shards": sent, "zones": zones}
    stages = (("w_rec_o", "w_out", "w_att_o_t"), ("w_ff1_t", "w_ff2"))

    def late_weights(after, stage):
        which = [later.index(n) for n in stages[stage]]
        travelling["shards"], travelling["zones"] = _gather_wait(
            send_sems, recv_sems, travelling["shards"], travelling["zones"], which, after,
            "weight_gather_wait_%d" % stage)
        return dict(zip(stages[stage], _gather_pass_on(
            [shard[n].shape[0] for n in stages[stage]], [travelling["zones"][i] for i in which],
            PASS_ON_IDS[stage], "weight_gather_pass_on_%d" % stage)))

    chan = _chan_from_bits(p.pop("chan"))
    r0 = 0
    for n, rows in CHAN:
        p[n] = chan[r0:r0 + rows]
        r0 += rows
    p.update(ln1_g=w["ln1_g"], b_in=w["b_in"] + token[0, 0], rpb=w["rpb"][0], conv_b=w["conv_b"],
             w_rg_a=w["w_rg_a"][0], w_rg_i=w["w_rg_i"][0], ln2_g=w["ln2_g"],
             lnf_g=w["lnf_g"].reshape(1, D))

    core = jnp.reshape(ci, (1,)).astype(jnp.int32)
    chip = jnp.reshape(2 * xi + yi, (1,)).astype(jnp.int32)
    first_sections = tuple(s for s in SECTIONS if s[0] in ("w_ff1_t", "w_ff2"))
    late_sections = SECTIONS[:1]
    early_sections = tuple(s for s in SECTIONS[1:] if s not in first_sections)
    in_flight = {}

    def pair_sum_and_send(group, sections, after):
        send_sems, recv_sems, sect, zones, _ = in_flight["pair_" + group]
        sect, got = _pair_exchange_wait(sections, send_sems, recv_sems, sect, zones, after,
                                        "grad_pair_exchange_wait_" + group)
        parts = _pair_add(sections, sect, got, core, "grad_pair_add_" + group)
        in_flight[group] = _chip_exchange_start(sections, parts, "grad_chip_exchange_start_" + group)
        return in_flight[group][-1]

    def pair_exchange_at_once(group, sections, grads, barrier_id):
        in_flight["pair_" + group] = _pair_exchange_start(
            sections, [grads[n] for n, _, _ in sections], barrier_id, "grad_pair_exchange_start_" + group)
        return pair_sum_and_send(group, sections, in_flight["pair_" + group][-1])

    def reduce_first(grads, after):
        if grads is None:
            return pair_sum_and_send("first", first_sections, after)
        in_flight["pair_first"] = _pair_exchange_start(
            first_sections, [grads[n] for n, _, _ in first_sections], PAIR_FIRST_ID,
            "grad_pair_exchange_start_first")
        return in_flight["pair_first"][-1]

    def reduce_early(grads):
        chan_g = jnp.concatenate([grads[n] for n, _ in CHAN], axis=0)
        chan_g = chan_g.reshape(N_CHAN_ROWS, N_DEV, LANES).transpose(1, 0, 2).astype(BF16)
        chan_g = jnp.pad(chan_g.reshape(N_DEV, -1), ((0, 0), (0, CHAN_BLOCK_ROWS * D - N_CHAN_ROWS * LANES)))
        grads["chan"] = chan_g.reshape(N_DEV * CHAN_BLOCK_ROWS, D)
        grads["gates"] = jnp.concatenate([grads[n].reshape(-1, D) for n in GATE_BLOCKS], axis=0).astype(BF16)
        return pair_exchange_at_once("early", early_sections, grads, PAIR_EARLY_ID)[0, 0]

    loss_part, grad_x, grads = _local_step(x[0], loss_target[0], p, late_weights, reduce_first, reduce_early)
    in_flight["pair_late"] = _pair_exchange_start(
        late_sections, [grads[n] for n, _, _ in late_sections], PAIR_LATE_ID, "grad_pair_exchange_start_late")

    def finish(group, sections, after, name):
        send_sems, recv_sems, parts, zones, _ = in_flight[group]
        parts, far = _chip_exchange_wait(sections, send_sems, recv_sems, parts, zones, after,
                                         "grad_chip_exchange_wait_" + name)
        return dict(zip((n for n, _, _ in sections),
                        _grad_finish(sections, parts, far, chip, "grad_finish_" + name)))

    summed = finish("first", first_sections, in_flight["pair_late"][-1], "first")
    summed.update(finish("early", early_sections, summed["w_ff2"], "early"))
    started_late = pair_sum_and_send("late", late_sections, summed["gates"])

    flat = jnp.concatenate([grads[n].reshape(-1) for n, _ in REPLICATED if n not in GATE_BLOCKS]
                           + [loss_part.reshape(-1) + started_late[0, 0]])
    n_small = flat.shape[0]
    flat = jnp.pad(flat, (0, SMALL_ROWS * LANES - n_small)).reshape(SMALL_ROWS, LANES)
    small_parts, gate_sum, _ = _all_gather([flat, summed["gates"]], "small_grad_all_gather")
    small = _sum_devices(small_parts, SMALL_ROWS, "small_grad_sum").reshape(-1)
    loss = small[n_small - 1]

    g, delta, new_m, new_v = {}, {}, {}, {}

    def update(n, g2, shape2):
        d2, m2, v2 = _adamw(w[n].reshape(shape2), g2, m[n].reshape(shape2), v[n].reshape(shape2),
                            "adamw_" + n)
        g[n], delta[n], new_m[n], new_v[n] = (a.reshape(w[n].shape) for a in (g2, d2, m2, v2))

    small_params = []
    o = 0
    for n, shape2 in REPLICATED:
        if n in GATE_BLOCKS:
            k, rows = GATE_BLOCKS.index(n), gate_sum.shape[0] // len(GATE_BLOCKS)
            update(n, gate_sum[k * rows:(k + 1) * rows].reshape(shape2), shape2)
        else:
            size = shape2[0] * shape2[1]
            small_params.append((n, small[o:o + size].reshape(shape2), shape2))
            o += size
    chan_back = summed["chan"].reshape(-1)[:N_CHAN_ROWS * LANES].reshape(N_CHAN_ROWS, LANES)
    r0 = 0
    for n, rows in CHAN:
        small_params.append((n, chan_back[r0:r0 + rows], (rows, LANES)))
        r0 += rows
    results = _adamw_small([(w[n].reshape(s2), g2, m[n].reshape(s2), v[n].reshape(s2))
                            for n, g2, s2 in small_params], "adamw_vectors")
    for (n, g2, _), (d2, m2, v2) in zip(small_params, results):
        g[n], delta[n], new_m[n], new_v[n] = (a.reshape(w[n].shape) for a in (g2, d2, m2, v2))

    for n in ROW_SHARDED:
        update(n, summed[n], summed[n].shape)
    for n, t in TRANSPOSED.items():
        if t in summed:
            update(n, summed[t].T, summed[t].shape[::-1])
    summed = finish("late", late_sections, _after_all(list(delta.values()), "updates_done"), "late")
    g_t = summed["w_in_t"]
    results = _adamw(w["w_in"][0].T, g_t, m["w_in"][0].T, v["w_in"][0].T, "adamw_w_in")
    g["w_in"], delta["w_in"], new_m["w_in"], new_v["w_in"] = (a.T[None] for a in (g_t, *results))

    return (loss, grad_x[None], *[g[n] for n in NAMES], *[delta[n] for n in NAMES],
            *[new_m[n] for n in NAMES], *[new_v[n] for n in NAMES])
```

```python
import math

import numpy as np
import jax
import jax.numpy as jnp
from jax import lax
from jax.experimental import pallas as pl
from jax.experimental.pallas import tpu as pltpu

F32 = jnp.float32
BF16 = jnp.bfloat16

T = 2048
D = 1024
D_ATT = 512
D_REC = 1024
D_FF = 4096
D_IN = 5632
N_HEADS = 8
DH = 64
GRID_W = 64
ROWS = T // GRID_W
WIN_H = 8
WIN_W = 16
KWIN = WIN_H * GRID_W
N_RPB_R = 2 * WIN_H - 1
N_RPB_C = 2 * WIN_W - 1
N_REC_BLOCKS = 16
REC_BLOCK = 64
CG = 128
N_CG = D_REC // CG
LRU_C = 8.0
EPS = 1e-6
N_DEV = 8
N_CHIPS = 4
LANES = 128

ADAM_LR = 0.001
ADAM_B1 = 0.9
ADAM_B2 = 0.999
ADAM_EPS = 1e-08
ADAM_WD = 0.01
ADAM_STEP = 10

MESH_AXES = ("x", "y", "c")
VMEM_LIMIT = 56 * 1024 * 1024

TILE = 512
DZ_ARRAYS = ((0, 3, 1), (3, 4, 2), (7, 4, 2))
N_DZ_TILES = D_IN // TILE


def _params(**kw):
    return pltpu.CompilerParams(vmem_limit_bytes=VMEM_LIMIT, **kw)


HG = 4
HQ = HG * GRID_W
HC = HG * DH


def _att_tables():
    rq = np.arange(GRID_W)
    kc = np.arange(KWIN) % GRID_W
    win_start = np.clip(rq - WIN_W // 2, 0, GRID_W - WIN_W)
    valid = (kc[None, :] >= win_start[:, None]) & (kc[None, :] < win_start[:, None] + WIN_W)
    same_head = (np.arange(HQ)[:, None] // GRID_W) == (np.arange(HC)[None, :] // DH)
    return valid.astype(np.float32), same_head.astype(np.float32)


def _pair_mask():
    half = np.arange(2 * DH) // DH
    return (half[:, None] == half[None, :]).astype(np.float32)


def _dup_table():
    return np.concatenate([np.eye(REC_BLOCK, dtype=np.float32)] * 2, axis=1)


def _sigmoid(x):
    return 0.5 * jnp.tanh(0.5 * x) + 0.5


def _softplus(x):
    return jnp.maximum(x, 0.0) + jnp.log(1.0 + jnp.exp(-jnp.abs(x)))


def _one_minus_square(log_a, a):
    x = 2.0 * log_a
    series = -x * (1.0 + x * (0.5 + x * (1.0 / 6.0)))
    return jnp.where(x > -0.02, series, 1.0 - a * a)


_GELU_C = math.sqrt(2.0 / math.pi)


def _gelu_and_grad(x):
    x2 = x * x
    inner = _GELU_C * (x + 0.044715 * x * x2)
    t = jnp.tanh(inner)
    g = 0.5 * x * (1.0 + t)
    dg = 0.5 * (1.0 + t) + 0.5 * x * (1.0 - t * t) * _GELU_C * (1.0 + 3.0 * 0.044715 * x2)
    return g, dg


def _dot(a, b):
    return jnp.dot(a, b, preferred_element_type=F32)


def _dot_nt(a, b):
    return lax.dot_general(a, b, (((1,), (1,)), ((), ())), preferred_element_type=F32)


def _dot_tn(a, b):
    return lax.dot_general(a, b, (((0,), (0,)), ((), ())), preferred_element_type=F32)


def _dot_exact(a, b):
    return jnp.dot(a, b, precision=lax.Precision.HIGHEST, preferred_element_type=F32)


def _shift_rows(x, s):
    n = x.shape[0]
    rows = lax.broadcasted_iota(jnp.int32, x.shape, 0)
    y = pltpu.roll(x, s % n, 0)
    if s > 0:
        return jnp.where(rows >= s, y, 0.0)
    return jnp.where(rows < n + s, y, 0.0)


def _rms_bwd(dh, xh, r, g):
    dxh = dh * g
    return r * (dxh - xh * jnp.mean(dxh * xh, axis=-1, keepdims=True))


def _matmul(a, b, mode, out_dtype, name, tm=512, tn=1024, tk=2048):
    if mode == "nn":
        (m, k), (k2, n) = a.shape, b.shape
    elif mode == "nt":
        (m, k), (n, k2) = a.shape, b.shape
    else:
        (k, m), (k2, n) = a.shape, b.shape
    assert k == k2
    tm, tn, tk = min(tm, m), min(tn, n), min(tk, k)
    assert m % tm == 0 and n % tn == 0 and k % tk == 0
    nk = k // tk
    dot = {"nn": _dot, "nt": _dot_nt, "tn": _dot_tn}[mode]

    def body(a_ref, b_ref, o_ref, acc):
        kk = pl.program_id(2)
        part = dot(a_ref[...].astype(BF16), b_ref[...].astype(BF16))
        if nk == 1:
            o_ref[...] = part.astype(out_dtype)
            return

        @pl.when(kk == 0)
        def _():
            acc[...] = part

        @pl.when(kk > 0)
        def _():
            acc[...] += part

        @pl.when(kk == nk - 1)
        def _():
            o_ref[...] = acc[...].astype(out_dtype)

    if mode == "tn":
        a_spec = pl.BlockSpec((tk, tm), lambda i, j, kk: (kk, i))
    else:
        a_spec = pl.BlockSpec((tm, tk), lambda i, j, kk: (i, kk))
    if mode == "nt":
        b_spec = pl.BlockSpec((tn, tk), lambda i, j, kk: (j, kk))
    else:
        b_spec = pl.BlockSpec((tk, tn), lambda i, j, kk: (kk, j))
    return pl.pallas_call(
        body, name=name,
        out_shape=jax.ShapeDtypeStruct((m, n), out_dtype),
        grid=(m // tm, n // tn, nk),
        in_specs=[a_spec, b_spec],
        out_specs=pl.BlockSpec((tm, tn), lambda i, j, kk: (i, j)),
        scratch_shapes=[pltpu.VMEM((tm, tn) if nk > 1 else (8, LANES), F32)],
        compiler_params=_params(dimension_semantics=("parallel", "parallel", "arbitrary")),
    )(a, b)


def _in_proj(x, g1, w_in_t, b_in):
    tm = 512

    def body(x_ref, g_ref, w_hbm, b_ref, qkv_ref, uy_ref, gg_ref, h_ref, w):
        @pl.when(pl.program_id(0) == 0)
        def _():
            pltpu.sync_copy(w_hbm, w)

        xv = x_ref[...]
        r = lax.rsqrt(jnp.mean(xv * xv, axis=-1, keepdims=True) + EPS)
        h = ((xv * r) * g_ref[...]).astype(BF16)
        h_ref[...] = h
        row0 = 0
        for ref in (qkv_ref, uy_ref, gg_ref):
            for c0 in range(0, ref.shape[1], TILE):
                z = _dot_nt(h, w[row0:row0 + TILE, :]) + b_ref[:, row0:row0 + TILE]
                ref[:, c0:c0 + TILE] = z.astype(ref.dtype)
                row0 += TILE

    tok = lambda width: pl.BlockSpec((tm, width), lambda i: (i, 0))
    return pl.pallas_call(
        body, name="in_proj",
        out_shape=(jax.ShapeDtypeStruct((T, 3 * D_ATT), BF16),
                   jax.ShapeDtypeStruct((T, 2 * D_REC), F32),
                   jax.ShapeDtypeStruct((T, 2 * D), F32),
                   jax.ShapeDtypeStruct((T, D), BF16)),
        grid=(T // tm,),
        in_specs=[tok(D), pl.BlockSpec((1, D), lambda i: (0, 0)), pl.BlockSpec(memory_space=pl.ANY),
                  pl.BlockSpec((1, D_IN), lambda i: (0, 0))],
        out_specs=(tok(3 * D_ATT), tok(2 * D_REC), tok(2 * D), tok(D)),
        scratch_shapes=[pltpu.VMEM((D_IN, D), BF16)],
        compiler_params=_params(dimension_semantics=("arbitrary",)),
    )(x, g1, w_in_t, b_in)


def _dz_specs(rows, tile_of, row_of):
    def spec(off, n, per_plane):
        def index(*ids):
            t = jnp.clip(tile_of(*ids) - off, 0, n - 1)
            return (t // per_plane, row_of(*ids), t % per_plane)
        return pl.BlockSpec((1, rows, TILE), index)
    return [spec(off, n, per) for off, n, per in DZ_ARRAYS]


def _dh_norm1_bwd(dz, w_in_t, x, g1, dx1):
    tm = 512

    def body(dqkv_ref, duy_ref, dgg_ref, w_hbm, x_ref, g_ref, dx1_ref, gx_ref, dg_ref, w):
        @pl.when(pl.program_id(0) == 0)
        def _():
            pltpu.sync_copy(w_hbm, w)
            dg_ref[...] = jnp.zeros_like(dg_ref)

        dh, row0 = None, 0
        for ref in (dqkv_ref, duy_ref, dgg_ref):
            for plane in range(ref.shape[0]):
                cols = ref.shape[2]
                part = _dot(ref[plane], w[row0:row0 + cols, :])
                dh = part if dh is None else dh + part
                row0 += cols
        xv = x_ref[...]
        r = lax.rsqrt(jnp.mean(xv * xv, axis=-1, keepdims=True) + EPS)
        xh = xv * r
        dg_ref[...] += jnp.sum(dh * xh, axis=0, keepdims=True)
        gx_ref[...] = dx1_ref[...] + _rms_bwd(dh, xh, r, g_ref[...])

    tok = pl.BlockSpec((tm, D), lambda i: (i, 0))
    vec = pl.BlockSpec((1, D), lambda i: (0, 0))
    planes = lambda a: pl.BlockSpec((a.shape[0], tm, a.shape[2]), lambda i: (0, i, 0))
    return pl.pallas_call(
        body, name="dh_norm1_bwd",
        out_shape=(jax.ShapeDtypeStruct((T, D), F32), jax.ShapeDtypeStruct((1, D), F32)),
        grid=(T // tm,),
        in_specs=[planes(a) for a in dz] + [pl.BlockSpec(memory_space=pl.ANY), tok, vec, tok],
        out_specs=(tok, vec),
        scratch_shapes=[pltpu.VMEM((D_IN, D), BF16)],
        compiler_params=_params(dimension_semantics=("arbitrary",)),
    )(*dz, w_in_t, x, g1, dx1)


def _grad_w_in(dz, h):
    def body(*refs):
        seg_refs = refs[:3]
        h_ref, gw_ref, gb_ref = refs[3:]
        j = pl.program_id(0)

        for s, (off, n, _) in enumerate(DZ_ARRAYS):
            @pl.when((j >= off) & (j < off + n))
            def _(s=s):
                a = seg_refs[s][0]
                gw_ref[...] = _dot_tn(a, h_ref[...]).astype(BF16)
                gb_ref[...] = jnp.sum(a.astype(F32), axis=0, keepdims=True)

    return pl.pallas_call(
        body, name="grad_w_in",
        out_shape=(jax.ShapeDtypeStruct((D_IN, D), BF16), jax.ShapeDtypeStruct((1, D_IN), F32)),
        grid=(N_DZ_TILES,),
        in_specs=_dz_specs(T, lambda j: j, lambda j: 0) + [pl.BlockSpec((T, D), lambda j: (0, 0))],
        out_specs=(pl.BlockSpec((TILE, D), lambda j: (j, 0)), pl.BlockSpec((1, TILE), lambda j: (0, j))),
        compiler_params=_params(dimension_semantics=("parallel",)),
    )(*dz, h)


def _rpb_rows(rpb):
    padded = jnp.pad(rpb, ((0, 0), (0, 0), (0, GRID_W - N_RPB_C)))
    rows = [padded[:, WIN_H - 1 - oi: 2 * WIN_H - 1 - oi].reshape(N_HEADS // HG, HG, KWIN)
            for oi in range(WIN_H)]
    return jnp.stack(rows, axis=0)


SKEW = KWIN - (WIN_W - 1)


MASKED = -1e30


def _bias_tiles(rows_ref, valid, bias_s):
    for oi in range(WIN_H):
        for hh in range(HG):
            row = jnp.broadcast_to(rows_ref[oi, 0, hh:hh + 1, :], (GRID_W, KWIN))
            tile = pltpu.roll(row, SKEW, 1, stride=1, stride_axis=0)
            bias_s[oi, hh * GRID_W:(hh + 1) * GRID_W, :] = jnp.where(valid, tile, MASKED)


def _bias_tile_grads(gb_s, flip, out_ref):
    for oi in range(WIN_H):
        for hh in range(HG):
            g = _dot_exact(flip, gb_s[oi, hh * GRID_W:(hh + 1) * GRID_W, :])
            back = pltpu.roll(g, KWIN - (GRID_W - WIN_W), 1, stride=1, stride_axis=0)
            out_ref[0, oi, hh:hh + 1, :] = jnp.sum(back, axis=0, keepdims=True)


def _rpb_fold(row_grads):
    g = row_grads.transpose(1, 0, 2, 3).reshape(WIN_H, N_HEADS, WIN_H, GRID_W)
    g = g.transpose(0, 2, 1, 3)

    def body(g_ref, o_ref):
        for dr in range(N_RPB_R):
            terms = [g_ref[oi, i] for oi in range(WIN_H) for i in range(WIN_H) if i - oi + WIN_H - 1 == dr]
            acc = terms[0]
            for term in terms[1:]:
                acc = acc + term
            o_ref[dr] = acc

    out = pl.pallas_call(
        body, name="rpb_fold",
        out_shape=jax.ShapeDtypeStruct((N_RPB_R, N_HEADS, GRID_W), F32),
    )(g)
    return out.transpose(1, 0, 2)[:, :, :N_RPB_C]


ATT_GROUPS = N_HEADS // HG
ATT_UNROLL = 8


def _stacked(rows64, same_head):
    return jnp.where(same_head, jnp.concatenate([rows64] * HG, axis=0), jnp.zeros((), BF16))


def _own_heads(stacked):
    head = lax.broadcasted_iota(jnp.int32, (GRID_W, HC), 1) // DH
    out = stacked[:GRID_W]
    for h in range(1, HG):
        out = jnp.where(head == h, stacked[h * GRID_W:(h + 1) * GRID_W], out)
    return out


def _att_scores(q_ref, k_ref, bias_ref, same_head, r):
    rs = jnp.clip(r - WIN_H // 2, 0, ROWS - WIN_H)
    oi = r - rs
    q0 = pl.multiple_of(r * GRID_W, GRID_W)
    k0 = pl.multiple_of(rs * GRID_W, GRID_W)
    q2 = _stacked(q_ref[pl.ds(q0, GRID_W), :] * (DH ** -0.5), same_head)
    kw = k_ref[pl.ds(k0, KWIN), :]
    s = _dot_nt(q2, kw) + bias_ref[oi]
    e = jnp.exp(s - jnp.max(s, axis=-1, keepdims=True))
    return e, 1.0 / jnp.sum(e, axis=-1, keepdims=True), q2, kw, q0, k0, oi


def _att_specs():
    col = lambda off: pl.BlockSpec((T, HC), lambda g: (0, g + off * ATT_GROUPS))
    tables = [pl.BlockSpec((WIN_H, 1, HG, KWIN), lambda g: (0, g, 0, 0)),
              pl.BlockSpec((GRID_W, KWIN), lambda g: (0, 0)),
              pl.BlockSpec((HQ, HC), lambda g: (0, 0))]
    return col, tables, pltpu.VMEM((WIN_H, HQ, KWIN), F32)


def _att_fwd(qkv, bias_rows):
    valid_np, same_head_np = _att_tables()

    def body(q_ref, k_ref, v_ref, rows_ref, valid_ref, head_ref, o_ref, bias_s):
        same_head = head_ref[...] > 0.5
        _bias_tiles(rows_ref, valid_ref[...] > 0.5, bias_s)

        def row(r, carry):
            e, rl, _, _, q0, k0, _ = _att_scores(q_ref, k_ref, bias_s, same_head, r)
            o2 = _dot((e * rl).astype(BF16), v_ref[pl.ds(k0, KWIN), :])
            o_ref[pl.ds(q0, GRID_W), :] = _own_heads(o2).astype(BF16)
            return carry

        lax.fori_loop(0, ROWS, row, 0, unroll=ATT_UNROLL)

    col, tables, tiles = _att_specs()
    return pl.pallas_call(
        body, name="att_fwd",
        out_shape=jax.ShapeDtypeStruct((T, D_ATT), BF16),
        grid=(ATT_GROUPS,),
        in_specs=[col(0), col(1), col(2)] + tables,
        out_specs=col(0),
        scratch_shapes=[tiles],
        compiler_params=_params(dimension_semantics=("parallel",)),
    )(qkv, qkv, qkv, bias_rows, jnp.asarray(valid_np), jnp.asarray(same_head_np))


def _att_bwd(qkv, bias_rows, datt, after):
    valid_np, same_head_np = _att_tables()

    def body(q_ref, k_ref, v_ref, do_ref, rows_ref, valid_ref, head_ref, flip_ref,
             dqkv_ref, grows_ref, dk_acc, dv_acc, bias_s, gb_s):
        same_head = head_ref[...] > 0.5
        dk_acc[...] = jnp.zeros_like(dk_acc)
        dv_acc[...] = jnp.zeros_like(dv_acc)
        gb_s[...] = jnp.zeros_like(gb_s)
        _bias_tiles(rows_ref, valid_ref[...] > 0.5, bias_s)

        def row(r, carry):
            e, rl, q2, kw, q0, k0, oi = _att_scores(q_ref, k_ref, bias_s, same_head, r)
            do2 = _stacked(do_ref[pl.ds(q0, GRID_W), :], same_head)
            vw = v_ref[pl.ds(k0, KWIN), :]
            p = e * rl
            dp = _dot_nt(do2, vw)
            ds = p * (dp - jnp.sum(dp * p, axis=-1, keepdims=True))
            p16 = p.astype(BF16)
            ds16 = ds.astype(BF16)
            dv_acc[pl.ds(k0, KWIN), :] += _dot_tn(p16, do2)
            dk_acc[pl.ds(k0, KWIN), :] += _dot_tn(ds16, q2)
            dq2 = _dot(ds16, kw) * (DH ** -0.5)
            dqkv_ref[0, pl.ds(q0, GRID_W), :] = _own_heads(dq2).astype(BF16)
            gb_s[oi] += ds
            return carry

        lax.fori_loop(0, ROWS, row, 0, unroll=ATT_UNROLL)
        dqkv_ref[1] = dk_acc[...].astype(BF16)
        dqkv_ref[2] = dv_acc[...].astype(BF16)
        _bias_tile_grads(gb_s, flip_ref[...], grows_ref)

    col, tables, tiles = _att_specs()
    return pl.pallas_call(
        body, name="att_bwd",
        out_shape=(jax.ShapeDtypeStruct((3, T, D_ATT), BF16),
                   jax.ShapeDtypeStruct((ATT_GROUPS, WIN_H, HG, KWIN), F32)),
        grid=(ATT_GROUPS,),
        in_specs=[col(0), col(1), col(2), col(0)] + tables + [pl.BlockSpec((GRID_W, GRID_W), lambda g: (0, 0))],
        out_specs=(pl.BlockSpec((3, T, HC), lambda g: (0, 0, g)),
                   pl.BlockSpec((1, WIN_H, HG, KWIN), lambda g: (g, 0, 0, 0))),
        scratch_shapes=[pltpu.VMEM((T, HC), F32), pltpu.VMEM((T, HC), F32), tiles, tiles],
        compiler_params=_params(dimension_semantics=("parallel",)),
    )(qkv, qkv, qkv, datt, bias_rows, jnp.asarray(valid_np) + after, jnp.asarray(same_head_np),
      jnp.asarray(np.eye(GRID_W, dtype=np.float32)[::-1].copy()))


def _conv_taps(up):
    return (_shift_rows(up, 2), _shift_rows(up, 1), up, _shift_rows(up, -1))


def _pair_block_diag(w_pair, dup, same_half):
    return jnp.where(same_half, _dot(w_pair.astype(BF16), dup), 0.0).astype(BF16)


def _gates(u, u16, wa, ba, wi, bi, lam):
    r = _sigmoid(_dot(u16, wa) + ba)
    ig = _sigmoid(_dot(u16, wi) + bi)
    sp = _softplus(-lam)
    log_a = (-LRU_C) * r * sp
    a = jnp.exp(log_a)
    mult2 = jnp.maximum(_one_minus_square(log_a, a), 0.0)
    return r, ig, sp, a, jnp.sqrt(mult2), mult2


SCAN_BLOCKS = 8


def _scans(jobs):
    c = jobs[0][0].shape[1]
    nblk = T // 8
    rows = lax.broadcasted_iota(jnp.int32, (8, c), 0)

    def block(a, b, reverse):
        for s in (1, 2, 4):
            if reverse:
                keep = rows < 8 - s
                a_s = jnp.where(keep, pltpu.roll(a, 8 - s, 0), 1.0)
                b_s = jnp.where(keep, pltpu.roll(b, 8 - s, 0), 0.0)
            else:
                keep = rows >= s
                a_s = jnp.where(keep, pltpu.roll(a, s, 0), 1.0)
                b_s = jnp.where(keep, pltpu.roll(b, s, 0), 0.0)
            b = a * b_s + b
            a = a * a_s
        return a, b

    def step(i, carry):
        out = []
        for (a_ref, b_ref, h_ref, reverse), h_prev in zip(jobs, carry):
            for u in range(SCAN_BLOCKS):
                blk = i * SCAN_BLOCKS + u
                if reverse:
                    blk = nblk - 1 - blk
                t0 = pl.multiple_of(blk * 8, 8)
                a, b = block(a_ref[pl.ds(t0, 8), :], b_ref[pl.ds(t0, 8), :], reverse)
                h = a * h_prev + b
                h_ref[pl.ds(t0, 8), :] = h
                h_prev = jnp.broadcast_to(h[0:1] if reverse else h[7:8], (8, c))
            out.append(h_prev)
        return tuple(out)

    lax.fori_loop(0, nblk // SCAN_BLOCKS, step, tuple(jnp.zeros((8, c), F32) for _ in jobs))


def _rec_specs():
    tok = lambda off: pl.BlockSpec((T, CG), lambda g: (0, g + off))
    per_ch = lambda rows: pl.BlockSpec((rows, CG), lambda g: (0, g))
    wspec = pl.BlockSpec((2, 1, CG, REC_BLOCK), lambda g: (0, g, 0, 0))
    const = lambda shape: pl.BlockSpec(shape, lambda g: (0, 0))
    return tok, per_ch, wspec, const


def _rec_fwd(uy, conv_w, conv_b, w_a, b_a, w_i, b_i, lam):
    tok, per_ch, wspec, const = _rec_specs()

    def body(up_ref, yb_ref, cw_ref, cb_ref, wa_ref, ba_ref, wi_ref, bi_ref, lam_ref, dup_ref, half_ref,
             hf_ref, hb_ref, yrec_ref, am_ref, bx_f, bx_b):
        dup = dup_ref[...]
        same_half = half_ref[...] > 0.5
        taps = _conv_taps(up_ref[...])
        u = cb_ref[...]
        for j in range(4):
            u = u + taps[j] * cw_ref[j:j + 1, :]
        u16 = u.astype(BF16)
        for d, bx_s in enumerate((bx_f, bx_b)):
            wa = _pair_block_diag(wa_ref[d, 0], dup, same_half)
            wi = _pair_block_diag(wi_ref[d, 0], dup, same_half)
            _, ig, _, a, mult, _ = _gates(u, u16, wa, ba_ref[d:d + 1, :], wi, bi_ref[d:d + 1, :],
                                       lam_ref[d:d + 1, :])
            am_ref[2 * d] = a
            am_ref[2 * d + 1] = mult
            bx_s[...] = mult * (ig * u)
        _scans([(am_ref.at[0], bx_f, hf_ref, False), (am_ref.at[2], bx_b, hb_ref, True)])
        gelu, _ = _gelu_and_grad(yb_ref[...])
        yrec_ref[...] = ((hf_ref[...] + hb_ref[...]) * gelu).astype(BF16)

    return pl.pallas_call(
        body, name="rec_fwd",
        out_shape=(jax.ShapeDtypeStruct((T, D_REC), F32), jax.ShapeDtypeStruct((T, D_REC), F32),
                   jax.ShapeDtypeStruct((T, D_REC), BF16), jax.ShapeDtypeStruct((4, T, D_REC), F32)),
        grid=(N_CG,),
        in_specs=[tok(0), tok(N_CG), per_ch(4), per_ch(1), wspec, per_ch(2), wspec, per_ch(2), per_ch(2),
                  const((REC_BLOCK, CG)), const((CG, CG))],
        out_specs=(tok(0), tok(0), tok(0), pl.BlockSpec((4, T, CG), lambda g: (0, 0, g))),
        scratch_shapes=[pltpu.VMEM((T, CG), F32)] * 2,
        compiler_params=_params(dimension_semantics=("parallel",)),
    )(uy, uy, conv_w, conv_b, w_a, b_a, w_i, b_i, lam,
      jnp.asarray(_dup_table(), BF16), jnp.asarray(_pair_mask()))


def _rec_bwd(uy, hf, hb, am, dyrec, conv_w, conv_b, w_a, b_a, w_i, b_i, lam):
    tok, per_ch, wspec, const = _rec_specs()

    def body(up_ref, yb_ref, hf_ref, hb_ref, am_ref, dy_ref, cw_ref, cb_ref, wa_ref, ba_ref, wi_ref, bi_ref,
             lam_ref, dup_ref, dupt_ref, half_ref,
             duy_ref, dcw_ref, dcb_ref, dwa_ref, dba_ref, dwi_ref, dbi_ref, dlam_ref,
             a_s0, a_s1, dh_s, g_s0, g_s1):
        dup = dup_ref[...]
        dup_t = dupt_ref[...]
        same_half = half_ref[...] > 0.5
        taps = _conv_taps(up_ref[...])
        u = cb_ref[...]
        for j in range(4):
            u = u + taps[j] * cw_ref[j:j + 1, :]
        u16 = u.astype(BF16)
        gelu, dgelu = _gelu_and_grad(yb_ref[...])
        dy = dy_ref[...]
        duy_ref[1] = (dy * (hf_ref[...] + hb_ref[...]) * dgelu).astype(BF16)
        dh_s[...] = dy * gelu
        a_s0[...] = _shift_rows(am_ref[0], -1)
        a_s1[...] = _shift_rows(am_ref[2], 1)
        _scans([(a_s0, dh_s, g_s0, True), (a_s1, dh_s, g_s1, False)])
        du = jnp.zeros((T, CG), F32)
        for d, g_s in enumerate((g_s0, g_s1)):
            reverse = d == 1
            wa = _pair_block_diag(wa_ref[d, 0], dup, same_half)
            wi = _pair_block_diag(wi_ref[d, 0], dup, same_half)
            lam_d = lam_ref[d:d + 1, :]
            r = _sigmoid(_dot(u16, wa) + ba_ref[d:d + 1, :])
            ig = _sigmoid(_dot(u16, wi) + bi_ref[d:d + 1, :])
            sp = _softplus(-lam_d)
            a, mult = am_ref[2 * d], am_ref[2 * d + 1]
            mult2 = mult * mult
            g = g_s[...]
            h_prev = _shift_rows(hb_ref[...], -1) if reverse else _shift_rows(hf_ref[...], 1)
            da = g * h_prev
            dmult = g * (ig * u)
            dig = g * mult * u
            du = du + g * mult * ig
            dmult_dlog = jnp.where(mult2 > 0.0, -(a * a) * lax.rsqrt(mult2), 0.0)
            dlog_a = da * a + dmult * dmult_dlog
            dr = dlog_a * ((-LRU_C) * sp)
            dsp = jnp.sum(dlog_a * ((-LRU_C) * r), axis=0, keepdims=True)
            dlam_ref[d:d + 1, :] = dsp * (-_sigmoid(-lam_d))
            dga = dr * r * (1.0 - r)
            dgi = dig * ig * (1.0 - ig)
            dga16 = dga.astype(BF16)
            dgi16 = dgi.astype(BF16)
            du = du + _dot_nt(dga16, wa) + _dot_nt(dgi16, wi)
            dwa_ref[d, 0] = _dot_exact(jnp.where(same_half, _dot_tn(u16, dga16), 0.0), dup_t)
            dwi_ref[d, 0] = _dot_exact(jnp.where(same_half, _dot_tn(u16, dgi16), 0.0), dup_t)
            dba_ref[d:d + 1, :] = jnp.sum(dga, axis=0, keepdims=True)
            dbi_ref[d:d + 1, :] = jnp.sum(dgi, axis=0, keepdims=True)
        dcb_ref[...] = jnp.sum(du, axis=0, keepdims=True)
        for j in range(4):
            dcw_ref[j:j + 1, :] = jnp.sum(du * taps[j], axis=0, keepdims=True)
        dup_in = (_shift_rows(du, -2) * cw_ref[0:1, :] + _shift_rows(du, -1) * cw_ref[1:2, :]
                  + du * cw_ref[2:3, :] + _shift_rows(du, 1) * cw_ref[3:4, :])
        duy_ref[0] = dup_in.astype(BF16)

    wshape = jax.ShapeDtypeStruct((2, N_CG, CG, REC_BLOCK), F32)
    vec = lambda rows: jax.ShapeDtypeStruct((rows, D_REC), F32)
    dup_np = _dup_table()
    return pl.pallas_call(
        body, name="rec_bwd",
        out_shape=(jax.ShapeDtypeStruct((2, T, D_REC), BF16),
                   vec(4), vec(1), wshape, vec(2), wshape, vec(2), vec(2)),
        grid=(N_CG,),
        in_specs=[tok(0), tok(N_CG), tok(0), tok(0), pl.BlockSpec((4, T, CG), lambda g: (0, 0, g)), tok(0),
                  per_ch(4), per_ch(1), wspec, per_ch(2), wspec, per_ch(2), per_ch(2),
                  const((REC_BLOCK, CG)), const((CG, REC_BLOCK)), const((CG, CG))],
        out_specs=(pl.BlockSpec((2, T, CG), lambda g: (0, 0, g)),
                   per_ch(4), per_ch(1), wspec, per_ch(2), wspec, per_ch(2), per_ch(2)),
        scratch_shapes=[pltpu.VMEM((T, CG), F32)] * 5,
        compiler_params=_params(dimension_semantics=("parallel",)),
    )(uy, uy, hf, hb, am, dyrec, conv_w, conv_b, w_a, b_a, w_i, b_i, lam,
      jnp.asarray(dup_np, BF16), jnp.asarray(dup_np.T.copy()), jnp.asarray(_pair_mask()))


TM_MIX = 256


def _mix_specs():
    tok = lambda width, blk=0: pl.BlockSpec((TM_MIX, width), lambda i: (i, blk))
    full = lambda shape: pl.BlockSpec(shape, lambda i: (0, 0))
    return tok, full


def _mix_fwd(x, att, yrec, gg, w_att_o_t, w_rec_o, w_out):
    tok, full = _mix_specs()

    def body(x_ref, att_ref, yr_ref, ga_ref, gr_ref, wao_ref, wro_ref, wo_ref, x1_ref, mixed_ref):
        y_att = _dot_nt(att_ref[...], wao_ref[...])
        y_rec = _dot(yr_ref[...], wro_ref[...])
        mixed = (_sigmoid(ga_ref[...]) * y_att + _sigmoid(gr_ref[...]) * y_rec).astype(BF16)
        mixed_ref[...] = mixed
        x1_ref[...] = x_ref[...] + _dot(mixed, wo_ref[...])

    return pl.pallas_call(
        body, name="mix_fwd",
        out_shape=(jax.ShapeDtypeStruct((T, D), F32), jax.ShapeDtypeStruct((T, D), BF16)),
        grid=(T // TM_MIX,),
        in_specs=[tok(D), tok(D_ATT), tok(D_REC), tok(D, 0), tok(D, 1),
                  full((D, D_ATT)), full((D_REC, D)), full((D, D))],
        out_specs=(tok(D), tok(D)),
        compiler_params=_params(dimension_semantics=("parallel",)),
    )(x, att, yrec, gg, gg, w_att_o_t, w_rec_o, w_out)


def _mix_bwd(dx1, att, yrec, gg, w_att_o_t, w_rec_o, w_out, after):
    tok, full = _mix_specs()

    def body(dx_ref, att_ref, yr_ref, ga_ref, gr_ref, wao_ref, wro_ref, wo_ref, after_ref,
             dgg_ref, dya_ref, dyr_ref, datt_ref, dyrp_ref):
        dmixed = _dot_nt(dx_ref[...].astype(BF16), wo_ref[...])
        y_att = _dot_nt(att_ref[...], wao_ref[...])
        y_rec = _dot(yr_ref[...], wro_ref[...])
        sa = _sigmoid(ga_ref[...])
        sr = _sigmoid(gr_ref[...])
        dgg_ref[0] = (dmixed * y_att * sa * (1.0 - sa)).astype(BF16)
        dgg_ref[1] = (dmixed * y_rec * sr * (1.0 - sr)).astype(BF16)
        dya = (dmixed * sa).astype(BF16)
        dyr = (dmixed * sr).astype(BF16)
        dya_ref[...] = dya
        dyr_ref[...] = dyr
        datt_ref[...] = _dot(dya, wao_ref[...]).astype(BF16)
        dyrp_ref[...] = _dot_nt(dyr, wro_ref[...])

    return pl.pallas_call(
        body, name="mix_bwd",
        out_shape=(jax.ShapeDtypeStruct((2, T, D), BF16),
                   jax.ShapeDtypeStruct((T, D), BF16), jax.ShapeDtypeStruct((T, D), BF16),
                   jax.ShapeDtypeStruct((T, D_ATT), BF16), jax.ShapeDtypeStruct((T, D_REC), F32)),
        grid=(T // TM_MIX,),
        in_specs=[tok(D), tok(D_ATT), tok(D_REC), tok(D, 0), tok(D, 1),
                  full((D, D_ATT)), full((D_REC, D)), full((D, D)), pl.BlockSpec(memory_space=pl.ANY)],
        out_specs=(pl.BlockSpec((2, TM_MIX, D), lambda i: (0, i, 0)),
                   tok(D), tok(D), tok(D_ATT), tok(D_REC)),
        compiler_params=_params(dimension_semantics=("parallel",)),
    )(dx1, att, yrec, gg, gg, w_att_o_t, w_rec_o, w_out, after)


TM_FFN = 256
FF_CHUNK = 1024


def _ffn_loss(x1, target, g2, gf, w_ff1_t, w_ff2):
    n_chunks = D_FF // FF_CHUNK

    def body(x1_ref, tg_ref, g2_ref, gf_ref, w1_hbm, w2_hbm,
             loss_ref, dx1_ref, h2_ref, act_ref, dpre_ref, dx2_ref, dg2_ref, dgf_ref,
             w1, w2, relu_s):
        i = pl.program_id(0)

        @pl.when(i == 0)
        def _():
            pltpu.sync_copy(w1_hbm, w1)
            pltpu.sync_copy(w2_hbm, w2)
            loss_ref[...] = jnp.zeros_like(loss_ref)
            dg2_ref[...] = jnp.zeros_like(dg2_ref)
            dgf_ref[...] = jnp.zeros_like(dgf_ref)

        x1v = x1_ref[...]
        r2 = lax.rsqrt(jnp.mean(x1v * x1v, axis=-1, keepdims=True) + EPS)
        xh2 = x1v * r2
        h2 = (xh2 * g2_ref[...]).astype(BF16)
        h2_ref[...] = h2
        x2 = x1v
        for c in range(n_chunks):
            ff = slice(c * FF_CHUNK, (c + 1) * FF_CHUNK)
            rl = jnp.maximum(_dot_nt(h2, w1[ff, :]), 0.0)
            relu_s[:, ff] = rl
            act = (rl * rl).astype(BF16)
            act_ref[:, ff] = act
            x2 = x2 + _dot(act, w2[ff, :])
        r3 = lax.rsqrt(jnp.mean(x2 * x2, axis=-1, keepdims=True) + EPS)
        xh3 = x2 * r3
        err = xh3 * gf_ref[...] - tg_ref[...]
        loss_ref[...] += 0.5 * jnp.sum(jnp.mean(err * err, axis=-1, keepdims=True))
        dy = err * (1.0 / D)
        dgf_ref[...] += jnp.sum(dy * xh3, axis=0, keepdims=True)
        dx2 = _rms_bwd(dy, xh3, r3, gf_ref[...])
        dx2_16 = dx2.astype(BF16)
        dx2_ref[...] = dx2_16
        dh2 = jnp.zeros((TM_FFN, D), F32)
        for c in range(n_chunks):
            ff = slice(c * FF_CHUNK, (c + 1) * FF_CHUNK)
            dpre = (_dot_nt(dx2_16, w2[ff, :]) * (2.0 * relu_s[:, ff])).astype(BF16)
            dpre_ref[:, ff] = dpre
            dh2 = dh2 + _dot(dpre, w1[ff, :])
        dg2_ref[...] += jnp.sum(dh2 * xh2, axis=0, keepdims=True)
        dx1_ref[...] = dx2 + _rms_bwd(dh2, xh2, r2, g2_ref[...])

    tok = lambda width: pl.BlockSpec((TM_FFN, width), lambda i: (i, 0))
    vec = pl.BlockSpec((1, D), lambda i: (0, 0))
    hbm = pl.BlockSpec(memory_space=pl.ANY)
    return pl.pallas_call(
        body, name="ffn_loss",
        out_shape=(jax.ShapeDtypeStruct((8, 128), F32), jax.ShapeDtypeStruct((T, D), F32),
                   jax.ShapeDtypeStruct((T, D), BF16), jax.ShapeDtypeStruct((T, D_FF), BF16),
                   jax.ShapeDtypeStruct((T, D_FF), BF16), jax.ShapeDtypeStruct((T, D), BF16),
                   jax.ShapeDtypeStruct((1, D), F32), jax.ShapeDtypeStruct((1, D), F32)),
        grid=(T // TM_FFN,),
        in_specs=[tok(D), tok(D), vec, vec, hbm, hbm],
        out_specs=(pl.BlockSpec((8, 128), lambda i: (0, 0)), tok(D), tok(D), tok(D_FF), tok(D_FF), tok(D),
                   vec, vec),
        scratch_shapes=[pltpu.VMEM((D_FF, D), BF16), pltpu.VMEM((D_FF, D), BF16),
                        pltpu.VMEM((TM_FFN, D_FF), F32)],
        compiler_params=_params(dimension_semantics=("arbitrary",)),
    )(x1, target, g2, gf, w_ff1_t, w_ff2)


def _local_step(x, target, p, late_weights, reduce_first, reduce_early):
    bias = _rpb_rows(p["rpb"])
    pairs = lambda w: w.reshape(2, N_CG, CG, REC_BLOCK)
    w_a, w_i = pairs(p["w_rg_a"]), pairs(p["w_rg_i"])
    rec_params = (p["conv_w"], p["conv_b"], w_a, p["b_rg_a"], w_i, p["b_rg_i"], p["lru_lambda"])

    qkv, uy, gg, h = _in_proj(x, p["ln1_g"], p["w_in_t"], p["b_in"])
    att = _att_fwd(qkv, bias)
    hf, hb, yrec, am = _rec_fwd(uy, *rec_params)
    p = {**p, **late_weights(yrec, 0)}
    x1, mixed = _mix_fwd(x, att, yrec, gg, p["w_att_o_t"], p["w_rec_o"], p["w_out"])
    p = {**p, **late_weights(x1, 1)}
    loss8, dx1, h2, act, dpre, dx2, g_ln2, g_lnf = _ffn_loss(
        x1, target, p["ln2_g"], p["lnf_g"], p["w_ff1_t"], p["w_ff2"])

    grads = {"ln2_g": g_ln2, "lnf_g": g_lnf,
             "w_ff1_t": _matmul(dpre, h2, "tn", BF16, "g_w_ff1"),
             "w_ff2": _matmul(act, dx2, "tn", BF16, "g_w_ff2")}
    dgg, dya, dyr, datt, dyrp = _mix_bwd(dx1, att, yrec, gg, p["w_att_o_t"], p["w_rec_o"], p["w_out"],
                                         reduce_first(grads, None))
    lam_after = rec_params[-1] + reduce_first(None, dgg)[0, 0]
    duy, g_cw, g_cb, g_wa, g_ba, g_wi, g_bi, g_lam = _rec_bwd(uy, hf, hb, am, dyrp, *rec_params[:-1], lam_after)
    blocks = lambda g: g.reshape(2, N_REC_BLOCKS, REC_BLOCK, REC_BLOCK)
    grads.update({
        "w_att_o_t": _matmul(dya, att, "tn", BF16, "g_w_att_o"),
        "conv_w": g_cw, "conv_b": g_cb, "w_rg_a": blocks(g_wa), "b_rg_a": g_ba,
        "w_rg_i": blocks(g_wi), "b_rg_i": g_bi, "lru_lambda": g_lam,
        "w_rec_o": _matmul(yrec, dyr, "tn", BF16, "g_w_rec_o"),
        "w_out": _matmul(mixed, dx1, "tn", BF16, "g_w_out"),
    })
    dqkv, gbias = _att_bwd(qkv, bias, datt, reduce_early(grads))
    dz = (dqkv, duy, dgg)
    grad_x, g_ln1 = _dh_norm1_bwd(dz, p["w_in_t"], x, p["ln1_g"], dx1)
    g_w_in_t, g_b_in = _grad_w_in(dz, h)
    grads.update(ln1_g=g_ln1, w_in_t=g_w_in_t, b_in=g_b_in, rpb=_rpb_fold(gbias))
    return loss8[0:1, 0:1], grad_x, grads


MESH_ID = pl.DeviceIdType.MESH
ANY = pl.BlockSpec(memory_space=pl.ANY)

CHAN_BLOCK_ROWS = 32
GATE_ROWS = 2 * 2 * N_REC_BLOCKS * REC_BLOCK * REC_BLOCK // (N_DEV * D)
SECTIONS = (("w_in_t", 704, D), ("w_rec_o", 128, D), ("w_out", 128, D), ("w_ff1_t", 512, D),
            ("w_ff2", 512, D), ("chan", CHAN_BLOCK_ROWS, D), ("w_att_o_t", 128, D_ATT),
            ("gates", GATE_ROWS, D))
N_SEC = len(SECTIONS)
N_CHAN_ROWS = 10
CHAN = (("conv_w", 4), ("b_rg_a", 2), ("b_rg_i", 2), ("lru_lambda", 2))


def _position():
    return lax.axis_index("x"), lax.axis_index("y"), lax.axis_index("c")


def _other_chips(x, y):
    return [(1 - x, y), (x, 1 - y), (1 - x, 1 - y)]


PASS_ON_IDS, PAIR_EARLY_ID, PAIR_LATE_ID, PAIR_FIRST_ID = (1, 4), 2, 3, 5


def _pair_handshake(x, y, c):
    barrier = pltpu.get_barrier_semaphore()
    pl.semaphore_signal(barrier, inc=1, device_id=(x, y, 1 - c), device_id_type=MESH_ID)
    pl.semaphore_wait(barrier, 1)


def _block_of(ref, dev, rows):
    return ref.at[pl.ds(pl.multiple_of(dev * rows, 16), rows)]


def _all_gather(shards, name):
    ns = len(shards)

    def body(*refs):
        x_refs, out_refs, done_ref = refs[:ns], refs[ns:2 * ns], refs[2 * ns]
        send_sems, recv_sems, local_sems = refs[2 * ns + 1:]
        done_ref[0, 0] = 0.0
        x, y, c = _position()
        me, sibling = (x, y, c), (x, y, 1 - c)
        x_nbr, y_nbr, diagonal = _other_chips(x, y)
        north = c == 1
        relay_from = (jnp.where(north, x_nbr[0], y_nbr[0]), jnp.where(north, x_nbr[1], y_nbr[1]))
        relay_to = (jnp.where(north, y_nbr[0], x_nbr[0]), jnp.where(north, y_nbr[1], x_nbr[1]))

        def rows(s, px, py, pc):
            return _block_of(out_refs[s], 4 * px + 2 * py + pc, shards[s].shape[0])

        def copy(k, s, block, to, from_shard=False):
            return pltpu.make_async_remote_copy(
                src_ref=x_refs[s] if from_shard else rows(s, *block), dst_ref=rows(s, *block),
                send_sem=send_sems.at[k * ns + s], recv_sem=recv_sems.at[k * ns + s],
                device_id=to, device_id_type=MESH_ID)

        sections = range(ns)
        mine = [pltpu.make_async_copy(x_refs[s], rows(s, *me), local_sems.at[s]) for s in sections]
        sent = [copy(k, s, me, to, True) for k, to in enumerate((sibling, (*x_nbr, c), (*y_nbr, c)))
                for s in sections]
        for cp in mine + sent:
            cp.start()
        for s in sections:
            copy(1, s, (*x_nbr, c), me).wait_recv()
            copy(2, s, (*y_nbr, c), me).wait_recv()
            sent += [copy(3, s, (*relay_from, c), (*relay_to, c)),
                     copy(4, s, (*x_nbr, c), sibling), copy(5, s, (*y_nbr, c), sibling)]
            for cp in sent[-3:]:
                cp.start()
        for s in sections:
            copy(3, s, (*diagonal, c), me).wait_recv()
            sent.append(copy(6, s, (*diagonal, c), sibling))
            sent[-1].start()
        for s in sections:
            copy(0, s, sibling, me).wait_recv()
            for k, chip in ((4, x_nbr), (5, y_nbr), (6, diagonal)):
                copy(k, s, (*chip, 1 - c), me).wait_recv()
        for cp in sent:
            cp.wait_send()
        for cp in mine:
            cp.wait()

    return pl.pallas_call(
        body, name=name,
        out_shape=tuple(jax.ShapeDtypeStruct((N_DEV * s.shape[0], s.shape[1]), s.dtype) for s in shards)
        + (jax.ShapeDtypeStruct((1, 1), F32),),
        in_specs=[ANY] * ns,
        out_specs=(ANY,) * ns + (pl.BlockSpec(memory_space=pltpu.SMEM),),
        scratch_shapes=[pltpu.SemaphoreType.DMA((7 * ns,)), pltpu.SemaphoreType.DMA((7 * ns,)),
                        pltpu.SemaphoreType.DMA((ns,))],
    )(*shards)


HBM = pl.BlockSpec(memory_space=pltpu.HBM)
SEM = pl.BlockSpec(memory_space=pltpu.SEMAPHORE)
EFFECT = pltpu.SideEffectType.DATAFLOW_SIDE_EFFECTING


def _in_hbm(a):
    return pltpu.with_memory_space_constraint(a, pltpu.HBM)


def _first_hop_copies(shards, x_refs, zones, send_sems, recv_sems):
    ns = len(shards)
    x, y, c = _position()
    targets = [(x, y, 1 - c)] + [(cx, cy, c) for cx, cy in _other_chips(x, y)]
    return [pltpu.make_async_remote_copy(
        src_ref=x_refs[s], dst_ref=_block_of(zones[s], 4 * x + 2 * y + c, shards[s].shape[0]),
        send_sem=send_sems.at[k * ns + s], recv_sem=recv_sems.at[k * ns + s],
        device_id=to, device_id_type=MESH_ID)
        for k, to in enumerate(targets) for s in range(ns)]


def _after_all(arrays, name):
    def body(*refs):
        refs[-1][...] = jnp.zeros_like(refs[-1])

    return pl.pallas_call(
        body, name=name,
        out_shape=jax.ShapeDtypeStruct((8, LANES), F32),
        in_specs=[pl.BlockSpec(memory_space=pl.ANY)] * len(arrays),
        out_specs=pl.BlockSpec(memory_space=pltpu.VMEM),
    )(*arrays)


def _own_blocks_placed(shards, after):
    ns = len(shards)
    x, y, c = _position()
    me = jnp.reshape(4 * x + 2 * y + c, (1,)).astype(jnp.int32)
    shards = [*shards[:-1], shards[-1] + after.astype(shards[-1].dtype)]

    def body(me_ref, *refs):
        for s in range(ns):
            refs[ns + s][...] = refs[s][...]

    return pl.pallas_call(
        body, name="own_blocks_placed",
        out_shape=tuple(jax.ShapeDtypeStruct((N_DEV * s.shape[0], s.shape[1]), s.dtype) for s in shards),
        grid_spec=pltpu.PrefetchScalarGridSpec(
            num_scalar_prefetch=1, grid=(1,),
            in_specs=[pl.BlockSpec(s.shape, lambda i, me: (0, 0)) for s in shards],
            out_specs=tuple(pl.BlockSpec(s.shape, lambda i, me: (me[0], 0)) for s in shards)),
        compiler_params=_params(dimension_semantics=("arbitrary",)),
    )(me, *shards)


def _gather_start(shards, after, name):
    ns = len(shards)
    zones = _own_blocks_placed(shards, after)

    def body(*refs):
        for cp in _first_hop_copies(shards, refs[:ns], refs[ns:2 * ns], refs[2 * ns], refs[2 * ns + 1]):
            cp.start()
        refs[-1][...] = jnp.zeros_like(refs[-1])

    out = pl.pallas_call(
        body, name=name,
        out_shape=(pltpu.SemaphoreType.DMA((4 * ns,)), pltpu.SemaphoreType.DMA((4 * ns,)),
                   *[pltpu.HBM(a.shape, a.dtype) for a in (*shards, *zones)],
                   jax.ShapeDtypeStruct((8, LANES), F32)),
        in_specs=[HBM] * (2 * ns),
        out_specs=(SEM, SEM, *[HBM] * (2 * ns), pl.BlockSpec(memory_space=pltpu.VMEM)),
        input_output_aliases={i: 2 + i for i in range(2 * ns)},
        compiler_params=pltpu.CompilerParams(has_side_effects=EFFECT),
    )(*[_in_hbm(a) for a in shards], *[_in_hbm(a) for a in zones])
    return out[0], out[1], out[2:2 + ns], out[2 + ns:2 + 2 * ns], out[-1]


def _gather_wait(send_sems, recv_sems, shards, zones, which, after, name):
    ns = len(shards)

    def body(*refs):
        copies = _first_hop_copies(shards, refs[:ns], refs[ns:2 * ns], refs[2 * ns], refs[2 * ns + 1])
        for i, cp in enumerate(copies):
            if i % ns in which:
                cp.wait_send()
                cp.wait_recv()

    out = pl.pallas_call(
        body, name=name,
        out_shape=tuple(pltpu.HBM(a.shape, a.dtype) for a in (*shards, *zones)),
        in_specs=[HBM] * (2 * ns) + [SEM, SEM, ANY],
        out_specs=(HBM,) * (2 * ns),
        input_output_aliases={i: i for i in range(2 * ns)},
        compiler_params=pltpu.CompilerParams(has_side_effects=EFFECT),
    )(*shards, *zones, send_sems, recv_sems, after)
    return out[:ns], out[ns:]


def _gather_pass_on(rows, zones, barrier_id, name):
    ns = len(zones)

    def body(*refs):
        in_refs, out_refs = refs[:ns], refs[ns:2 * ns]
        send_sems, recv_sems = refs[2 * ns:]
        x, y, c = _position()
        _pair_handshake(x, y, c)
        copies = [pltpu.make_async_remote_copy(
            src_ref=_block_of(in_refs[s], 4 * cx + 2 * cy + c, rows[s]),
            dst_ref=_block_of(out_refs[s], 4 * cx + 2 * cy + c, rows[s]),
            send_sem=send_sems.at[j * ns + s], recv_sem=recv_sems.at[j * ns + s],
            device_id=(x, y, 1 - c), device_id_type=MESH_ID)
            for j, (cx, cy) in enumerate(_other_chips(x, y)) for s in range(ns)]
        for cp in copies:
            cp.start()
        for cp in copies:
            cp.wait_recv()
        for cp in copies:
            cp.wait_send()

    return pl.pallas_call(
        body, name=name,
        out_shape=tuple(jax.ShapeDtypeStruct(z.shape, z.dtype) for z in zones),
        in_specs=[ANY] * ns, out_specs=(ANY,) * ns,
        input_output_aliases={i: i for i in range(ns)},
        scratch_shapes=[pltpu.SemaphoreType.DMA((3 * ns,)), pltpu.SemaphoreType.DMA((3 * ns,))],
        compiler_params=pltpu.CompilerParams(collective_id=barrier_id),
    )(*zones)


def _pair_copies(sections, g_refs, land, send_sems, recv_sems):
    ns = len(sections)
    x, y, c = _position()
    return [pltpu.make_async_remote_copy(
        src_ref=_block_of(g_refs[s], 2 * k + 1 - c, rows), dst_ref=land[s].at[k],
        send_sem=send_sems.at[k * ns + s], recv_sem=recv_sems.at[k * ns + s],
        device_id=(x, y, 1 - c), device_id_type=MESH_ID)
        for k in range(N_CHIPS) for s, (_, rows, _) in enumerate(sections)]


def _pair_exchange_start(sections, grads, barrier_id, name):
    ns = len(sections)

    def body(*refs):
        _pair_handshake(*_position())
        for cp in _pair_copies(sections, refs[:ns], refs[ns:2 * ns], refs[2 * ns], refs[2 * ns + 1]):
            cp.start()
        refs[-1][...] = jnp.zeros_like(refs[-1])

    zones = [lax.empty((N_CHIPS, rows, cols), BF16) for _, rows, cols in sections]
    n = N_CHIPS * ns
    out = pl.pallas_call(
        body, name=name,
        out_shape=(pltpu.SemaphoreType.DMA((n,)), pltpu.SemaphoreType.DMA((n,)),
                   *[pltpu.HBM(a.shape, a.dtype) for a in (*grads, *zones)],
                   jax.ShapeDtypeStruct((8, LANES), F32)),
        in_specs=[HBM] * (2 * ns),
        out_specs=(SEM, SEM, *[HBM] * (2 * ns), pl.BlockSpec(memory_space=pltpu.VMEM)),
        input_output_aliases={i: 2 + i for i in range(2 * ns)},
        compiler_params=pltpu.CompilerParams(has_side_effects=EFFECT, collective_id=barrier_id),
    )(*[_in_hbm(a) for a in grads], *[_in_hbm(a) for a in zones])
    return out[0], out[1], out[2:2 + ns], out[2 + ns:2 + 2 * ns], out[-1]


def _pair_exchange_wait(sections, send_sems, recv_sems, grads, zones, after, name):
    ns = len(sections)

    def body(*refs):
        for cp in _pair_copies(sections, refs[:ns], refs[ns:2 * ns], refs[2 * ns], refs[2 * ns + 1]):
            cp.wait_send()
            cp.wait_recv()

    out = pl.pallas_call(
        body, name=name,
        out_shape=tuple(pltpu.HBM(a.shape, a.dtype) for a in (*grads, *zones)),
        in_specs=[HBM] * (2 * ns) + [SEM, SEM, ANY],
        out_specs=(HBM,) * (2 * ns),
        input_output_aliases={i: i for i in range(2 * ns)},
        compiler_params=pltpu.CompilerParams(has_side_effects=EFFECT),
    )(*grads, *zones, send_sems, recv_sems, after)
    return out[:ns], out[ns:]


def _pair_add(sections, grads, got, core, name):
    ns = len(sections)

    def body(core_ref, *refs):
        g_refs, got_refs, p_refs = refs[:ns], refs[ns:2 * ns], refs[2 * ns:]
        for s in range(ns):
            p_refs[s][0] = (g_refs[s][...].astype(F32) + got_refs[s][0].astype(F32)).astype(BF16)

    slot = [pl.BlockSpec((1, rows, cols), lambda k, c: (k, 0, 0)) for _, rows, cols in sections]
    return pl.pallas_call(
        body, name=name,
        out_shape=tuple(jax.ShapeDtypeStruct((N_CHIPS, rows, cols), BF16) for _, rows, cols in sections),
        grid_spec=pltpu.PrefetchScalarGridSpec(
            num_scalar_prefetch=1, grid=(N_CHIPS,),
            in_specs=[pl.BlockSpec((rows, cols), lambda k, c: (2 * k + c[0], 0)) for _, rows, cols in sections]
            + slot,
            out_specs=tuple(slot)),
        compiler_params=_params(dimension_semantics=("parallel",)),
    )(core, *grads, *got)


def _chip_copies(sections, p_refs, land, send_sems, recv_sems):
    ns = len(sections)
    x, y, c = _position()
    return [pltpu.make_async_remote_copy(
        src_ref=p_refs[s].at[2 * cx + cy], dst_ref=land[s].at[j],
        send_sem=send_sems.at[j * ns + s], recv_sem=recv_sems.at[j * ns + s],
        device_id=(cx, cy, c), device_id_type=MESH_ID)
        for j, (cx, cy) in enumerate(_other_chips(x, y)) for s in range(ns)]


def _chip_exchange(sections, parts, name):
    ns = len(sections)

    def body(*refs):
        copies = _chip_copies(sections, refs[:ns], refs[ns:2 * ns], *refs[2 * ns:])
        for cp in copies:
            cp.start()
        for cp in copies:
            cp.wait_recv()
        for cp in copies:
            cp.wait_send()

    n = 3 * ns
    return pl.pallas_call(
        body, name=name,
        out_shape=tuple(jax.ShapeDtypeStruct((3, rows, cols), BF16) for _, rows, cols in sections),
        in_specs=[ANY] * ns, out_specs=(ANY,) * ns,
        scratch_shapes=[pltpu.SemaphoreType.DMA((n,)), pltpu.SemaphoreType.DMA((n,))],
    )(*parts)


def _chip_exchange_start(sections, parts, name):
    ns = len(sections)

    def body(*refs):
        p_refs, land = refs[:ns], refs[ns:2 * ns]
        send_sems, recv_sems = refs[2 * ns], refs[2 * ns + 1]
        token = refs[-1]
        for cp in _chip_copies(sections, p_refs, land, send_sems, recv_sems):
            cp.start()
        token[...] = jnp.zeros_like(token)

    zones = [lax.empty((3, rows, cols), BF16) for _, rows, cols in sections]
    out = pl.pallas_call(
        body, name=name,
        out_shape=(pltpu.SemaphoreType.DMA((3 * ns,)), pltpu.SemaphoreType.DMA((3 * ns,)),
                   *[pltpu.HBM(a.shape, a.dtype) for a in parts], *[pltpu.HBM(a.shape, a.dtype) for a in zones],
                   jax.ShapeDtypeStruct((8, LANES), F32)),
        in_specs=[HBM] * (2 * ns),
        out_specs=(SEM, SEM, *[HBM] * (2 * ns), pl.BlockSpec(memory_space=pltpu.VMEM)),
        input_output_aliases={i: 2 + i for i in range(2 * ns)},
        compiler_params=pltpu.CompilerParams(has_side_effects=EFFECT),
    )(*[_in_hbm(a) for a in parts], *[_in_hbm(a) for a in zones])
    return out[0], out[1], out[2:2 + ns], out[2 + ns:2 + 2 * ns], out[-1]


def _chip_exchange_wait(sections, send_sems, recv_sems, parts, zones, after, name):
    ns = len(sections)

    def body(*refs):
        p_refs, land = refs[:ns], refs[ns:2 * ns]
        for cp in _chip_copies(sections, p_refs, land, refs[2 * ns], refs[2 * ns + 1]):
            cp.wait_send()
            cp.wait_recv()

    out = pl.pallas_call(
        body, name=name,
        out_shape=tuple(pltpu.HBM(a.shape, a.dtype) for a in (*parts, *zones)),
        in_specs=[HBM] * (2 * ns) + [SEM, SEM, ANY],
        out_specs=(HBM,) * (2 * ns),
        input_output_aliases={i: i for i in range(2 * ns)},
        compiler_params=pltpu.CompilerParams(has_side_effects=EFFECT),
    )(*parts, *zones, send_sems, recv_sems, after)
    return out[:ns], out[ns:]


def _grad_finish(sections, parts, far, chip, name):
    ns = len(sections)

    def body(chip_ref, *refs):
        p_refs, b_refs, g_refs = refs[:ns], refs[ns:2 * ns], refs[2 * ns:]
        for s in range(ns):
            g = p_refs[s][0].astype(F32)
            for j in range(3):
                g = g + b_refs[s][j].astype(F32)
            g_refs[s][...] = g

    half = [(rows // 2, cols) for _, rows, cols in sections]
    return pl.pallas_call(
        body, name=name,
        out_shape=tuple(jax.ShapeDtypeStruct((rows, cols), F32) for _, rows, cols in sections),
        grid_spec=pltpu.PrefetchScalarGridSpec(
            num_scalar_prefetch=1, grid=(2,),
            in_specs=[pl.BlockSpec((1, r, c), lambda i, chip: (chip[0], i, 0)) for r, c in half]
            + [pl.BlockSpec((3, r, c), lambda i, chip: (0, i, 0)) for r, c in half],
            out_specs=tuple(pl.BlockSpec((r, c), lambda i, chip: (i, 0)) for r, c in half)),
        compiler_params=_params(dimension_semantics=("parallel",)),
    )(chip, *parts, *far)


def _sum_devices(parts, rows, name):
    cols = parts.shape[1]
    tr = rows // 2

    def body(*refs):
        s = refs[0][...].astype(F32)
        for d in range(1, N_DEV):
            s = s + refs[d][...].astype(F32)
        refs[N_DEV][...] = s

    return pl.pallas_call(
        body, name=name,
        out_shape=jax.ShapeDtypeStruct((rows, cols), F32),
        grid=(2,),
        in_specs=[pl.BlockSpec((tr, cols), lambda i, d=d: (2 * d + i, 0)) for d in range(N_DEV)],
        out_specs=pl.BlockSpec((tr, cols), lambda i: (i, 0)),
        compiler_params=_params(dimension_semantics=("parallel",)),
    )(*([parts] * N_DEV))


def _adamw_step(w_ref, g_ref, m_ref, v_ref, d_ref, nm_ref, nv_ref):
    c1 = 1.0 / (1.0 - ADAM_B1 ** ADAM_STEP)
    c2 = 1.0 / (1.0 - ADAM_B2 ** ADAM_STEP)
    gv = g_ref[...]
    nm = ADAM_B1 * m_ref[...] + (1.0 - ADAM_B1) * gv
    nv = ADAM_B2 * v_ref[...] + (1.0 - ADAM_B2) * (gv * gv)
    nm_ref[...] = nm
    nv_ref[...] = nv
    d_ref[...] = (-ADAM_LR) * ((nm * c1) / (jnp.sqrt(nv * c2) + ADAM_EPS) + ADAM_WD * w_ref[...])


def _adamw_small(params, name):
    n = len(params)

    def body(*refs):
        for k in range(n):
            _adamw_step(*refs[4 * k:4 * k + 4], *refs[4 * n + 3 * k:4 * n + 3 * k + 3])

    out = pl.pallas_call(
        body, name=name,
        out_shape=tuple(jax.ShapeDtypeStruct(p[0].shape, F32) for p in params for _ in range(3)),
    )(*[a for p in params for a in p])
    return [out[3 * k:3 * k + 3] for k in range(n)]


def _adamw(w, g, m, v, name):
    rows, cols = w.shape
    tr = rows
    while tr * cols * 4 > (1 << 20) and tr % 16 == 0:
        tr //= 2

    def body(*refs):
        _adamw_step(*refs)

    spec = pl.BlockSpec((tr, cols), lambda i: (i, 0))
    shape = jax.ShapeDtypeStruct((rows, cols), F32)
    return pl.pallas_call(
        body, name=name,
        out_shape=(shape, shape, shape),
        grid=(rows // tr,),
        in_specs=[spec] * 4, out_specs=(spec,) * 3,
        compiler_params=_params(dimension_semantics=("parallel",)),
    )(w, g, m, v)


NAMES = ("ln1_g", "w_in", "b_in", "rpb", "w_att_o", "conv_w", "conv_b", "w_rg_a", "b_rg_a", "w_rg_i",
         "b_rg_i", "lru_lambda", "w_rec_o", "w_out", "ln2_g", "w_ff1", "w_ff2", "lnf_g")
TRANSPOSED = {"w_in": "w_in_t", "w_att_o": "w_att_o_t", "w_ff1": "w_ff1_t"}
ROW_SHARDED = ("w_rec_o", "w_out", "w_ff2")
REPLICATED = (("ln1_g", (1, D)), ("b_in", (1, D_IN)), ("rpb", (N_HEADS * N_RPB_R, N_RPB_C)),
              ("conv_b", (1, D_REC)), ("w_rg_a", (2 * N_REC_BLOCKS * REC_BLOCK, REC_BLOCK)),
              ("w_rg_i", (2 * N_REC_BLOCKS * REC_BLOCK, REC_BLOCK)), ("ln2_g", (1, D)), ("lnf_g", (1, D)))
GATE_BLOCKS = ("w_rg_a", "w_rg_i")
SMALL_ROWS = 112


def _chan_bits(vectors):
    chan = jnp.concatenate(vectors, axis=0)
    bits = lax.bitcast_convert_type(chan, BF16).reshape(-1)
    return jnp.pad(bits, (0, CHAN_BLOCK_ROWS * D - bits.shape[0])).reshape(CHAN_BLOCK_ROWS, D)


def _chan_from_bits(gathered):
    bits = gathered.reshape(N_DEV, CHAN_BLOCK_ROWS * D)[:, :2 * N_CHAN_ROWS * LANES]
    chan = lax.bitcast_convert_type(bits.reshape(N_DEV, N_CHAN_ROWS, LANES, 2), F32)
    return chan.transpose(1, 0, 2).reshape(N_CHAN_ROWS, D)


def kernel(x, ln1_g, w_in, b_in, rpb, w_att_o, conv_w, conv_b, w_rg_a, b_rg_a, w_rg_i, b_rg_i, lru_lambda, w_rec_o, w_out, ln2_g, w_ff1, w_ff2, lnf_g, loss_target, m_ln1_g, m_w_in, m_b_in, m_rpb, m_w_att_o, m_conv_w, m_conv_b, m_w_rg_a, m_b_rg_a, m_w_rg_i, m_b_rg_i, m_lru_lambda, m_w_rec_o, m_w_out, m_ln2_g, m_w_ff1, m_w_ff2, m_lnf_g, v_ln1_g, v_w_in, v_b_in, v_rpb, v_w_att_o, v_conv_w, v_conv_b, v_w_rg_a, v_b_rg_a, v_w_rg_i, v_b_rg_i, v_lru_lambda, v_w_rec_o, v_w_out, v_ln2_g, v_w_ff1, v_w_ff2, v_lnf_g):
    w = dict(zip(NAMES, (ln1_g, w_in, b_in, rpb, w_att_o, conv_w, conv_b, w_rg_a, b_rg_a, w_rg_i,
                         b_rg_i, lru_lambda, w_rec_o, w_out, ln2_g, w_ff1, w_ff2, lnf_g)))
    m = dict(zip(NAMES, (m_ln1_g, m_w_in, m_b_in, m_rpb, m_w_att_o, m_conv_w, m_conv_b, m_w_rg_a,
                         m_b_rg_a, m_w_rg_i, m_b_rg_i, m_lru_lambda, m_w_rec_o, m_w_out, m_ln2_g,
                         m_w_ff1, m_w_ff2, m_lnf_g)))
    v = dict(zip(NAMES, (v_ln1_g, v_w_in, v_b_in, v_rpb, v_w_att_o, v_conv_w, v_conv_b, v_w_rg_a,
                         v_b_rg_a, v_w_rg_i, v_b_rg_i, v_lru_lambda, v_w_rec_o, v_w_out, v_ln2_g,
                         v_w_ff1, v_w_ff2, v_lnf_g)))
    xi, yi, ci = _position()

    shard = {t: w[n][0].T.astype(BF16) for n, t in TRANSPOSED.items()}
    shard.update({n: w[n][0].astype(BF16) for n in ROW_SHARDED})
    shard["chan"] = _chan_bits([w[n][0] for n, _ in CHAN])
    first, later = ("w_in_t", "chan"), ("w_rec_o", "w_out", "w_att_o_t", "w_ff1_t", "w_ff2")
    *gathered, done = _all_gather([shard[n] for n in first], "weight_all_gather")
    p = dict(zip(first, gathered))
    send_sems, recv_sems, sent, zones, token = _gather_start([shard[n] for n in later], done,
                                                             "weight_gather_start")

    travelling = {"shards": sent, "zones": zones}
    stages = (("w_rec_o", "w_out", "w_att_o_t"), ("w_ff1_t", "w_ff2"))

    def late_weights(after, stage):
        which = [later.index(n) for n in stages[stage]]
        travelling["shards"], travelling["zones"] = _gather_wait(
            send_sems, recv_sems, travelling["shards"], travelling["zones"], which, after,
            "weight_gather_wait_%d" % stage)
        return dict(zip(stages[stage], _gather_pass_on(
            [shard[n].shape[0] for n in stages[stage]], [travelling["zones"][i] for i in which],
            PASS_ON_IDS[stage], "weight_gather_pass_on_%d" % stage)))

    chan = _chan_from_bits(p.pop("chan"))
    r0 = 0
    for n, rows in CHAN:
        p[n] = chan[r0:r0 + rows]
        r0 += rows
    p.update(ln1_g=w["ln1_g"], b_in=w["b_in"] + token[0, 0], rpb=w["rpb"][0], conv_b=w["conv_b"],
             w_rg_a=w["w_rg_a"][0], w_rg_i=w["w_rg_i"][0], ln2_g=w["ln2_g"],
             lnf_g=w["lnf_g"].reshape(1, D))

    core = jnp.reshape(ci, (1,)).astype(jnp.int32)
    chip = jnp.reshape(2 * xi + yi, (1,)).astype(jnp.int32)
    first_sections = tuple(s for s in SECTIONS if s[0] in ("w_ff1_t", "w_ff2"))
    late_sections = SECTIONS[:1]
    early_sections = tuple(s for s in SECTIONS[1:] if s not in first_sections)
    in_flight = {}

    def pair_sum_and_send(group, sections, after):
        send_sems, recv_sems, sect, zones, _ = in_flight["pair_" + group]
        sect, got = _pair_exchange_wait(sections, send_sems, recv_sems, sect, zones, after,
                                        "grad_pair_exchange_wait_" + group)
        parts = _pair_add(sections, sect, got, core, "grad_pair_add_" + group)
        in_flight[group] = _chip_exchange_start(sections, parts, "grad_chip_exchange_start_" + group)
        return in_flight[group][-1]

    def pair_exchange_at_once(group, sections, grads, barrier_id):
        in_flight["pair_" + group] = _pair_exchange_start(
            sections, [grads[n] for n, _, _ in sections], barrier_id, "grad_pair_exchange_start_" + group)
        return pair_sum_and_send(group, sections, in_flight["pair_" + group][-1])

    def reduce_first(grads, after):
        if grads is None:
            return pair_sum_and_send("first", first_sections, after)
        in_flight["pair_first"] = _pair_exchange_start(
            first_sections, [grads[n] for n, _, _ in first_sections], PAIR_FIRST_ID,
            "grad_pair_exchange_start_first")
        return in_flight["pair_first"][-1]

    def reduce_early(grads):
        chan_g = jnp.concatenate([grads[n] for n, _ in CHAN], axis=0)
        chan_g = chan_g.reshape(N_CHAN_ROWS, N_DEV, LANES).transpose(1, 0, 2).astype(BF16)
        chan_g = jnp.pad(chan_g.reshape(N_DEV, -1), ((0, 0), (0, CHAN_BLOCK_ROWS * D - N_CHAN_ROWS * LANES)))
        grads["chan"] = chan_g.reshape(N_DEV * CHAN_BLOCK_ROWS, D)
        grads["gates"] = jnp.concatenate([grads[n].reshape(-1, D) for n in GATE_BLOCKS], axis=0).astype(BF16)
        return pair_exchange_at_once("early", early_sections, grads, PAIR_EARLY_ID)[0, 0]

    loss_part, grad_x, grads = _local_step(x[0], loss_target[0], p, late_weights, reduce_first, reduce_early)
    in_flight["pair_late"] = _pair_exchange_start(
        late_sections, [grads[n] for n, _, _ in late_sections], PAIR_LATE_ID, "grad_pair_exchange_start_late")

    def finish(group, sections, after, name):
        send_sems, recv_sems, parts, zones, _ = in_flight[group]
        parts, far = _chip_exchange_wait(sections, send_sems, recv_sems, parts, zones, after,
                                         "grad_chip_exchange_wait_" + name)
        return dict(zip((n for n, _, _ in sections),
                        _grad_finish(sections, parts, far, chip, "grad_finish_" + name)))

    summed = finish("first", first_sections, in_flight["pair_late"][-1], "first")
    summed.update(finish("early", early_sections, summed["w_ff2"], "early"))

    flat = jnp.concatenate([grads[n].reshape(-1) for n, _ in REPLICATED if n not in GATE_BLOCKS]
                           + [loss_part.reshape(-1)])
    n_small = flat.shape[0]
    flat = jnp.pad(flat, (0, SMALL_ROWS * LANES - n_small)).reshape(SMALL_ROWS, LANES)
    small_parts, gate_sum, _ = _all_gather([flat, summed["gates"]], "small_grad_all_gather")
    started_late = pair_sum_and_send("late", late_sections, small_parts)
    small = _sum_devices(small_parts + started_late[0, 0], SMALL_ROWS, "small_grad_sum").reshape(-1)
    loss = small[n_small - 1]

    g, delta, new_m, new_v = {}, {}, {}, {}

    def update(n, g2, shape2):
        d2, m2, v2 = _adamw(w[n].reshape(shape2), g2, m[n].reshape(shape2), v[n].reshape(shape2),
                            "adamw_" + n)
        g[n], delta[n], new_m[n], new_v[n] = (a.reshape(w[n].shape) for a in (g2, d2, m2, v2))

    small_params = []
    o = 0
    for n, shape2 in REPLICATED:
        if n in GATE_BLOCKS:
            k, rows = GATE_BLOCKS.index(n), gate_sum.shape[0] // len(GATE_BLOCKS)
            update(n, gate_sum[k * rows:(k + 1) * rows].reshape(shape2), shape2)
        else:
            size = shape2[0] * shape2[1]
            small_params.append((n, small[o:o + size].reshape(shape2), shape2))
            o += size
    chan_back = summed["chan"].reshape(-1)[:N_CHAN_ROWS * LANES].reshape(N_CHAN_ROWS, LANES)
    r0 = 0
    for n, rows in CHAN:
        small_params.append((n, chan_back[r0:r0 + rows], (rows, LANES)))
        r0 += rows
    results = _adamw_small([(w[n].reshape(s2), g2, m[n].reshape(s2), v[n].reshape(s2))
                            for n, g2, s2 in small_params], "adamw_vectors")
    for (n, g2, _), (d2, m2, v2) in zip(small_params, results):
        g[n], delta[n], new_m[n], new_v[n] = (a.reshape(w[n].shape) for a in (g2, d2, m2, v2))

    for n in ROW_SHARDED:
        update(n, summed[n], summed[n].shape)
    for n, t in TRANSPOSED.items():
        if t in summed:
            update(n, summed[t].T, summed[t].shape[::-1])
    summed = finish("late", late_sections, _after_all(list(delta.values()), "updates_done"), "late")
    g_t = summed["w_in_t"]
    results = _adamw(w["w_in"][0].T, g_t, m["w_in"][0].T, v["w_in"][0].T, "adamw_w_in")
    g["w_in"], delta["w_in"], new_m["w_in"], new_v["w_in"] = (a.T[None] for a in (g_t, *results))

    return (loss, grad_x[None], *[g[n] for n in NAMES], *[delta[n] for n in NAMES],
            *[new_m[n] for n in NAMES], *[new_v[n] for n in NAMES])
```

```python
import math

import numpy as np
import jax
import jax.numpy as jnp
from jax import lax
from jax.experimental import pallas as pl
from jax.experimental.pallas import tpu as pltpu

F32 = jnp.float32
BF16 = jnp.bfloat16

T = 2048
D = 1024
D_ATT = 512
D_REC = 1024
D_FF = 4096
D_IN = 5632
N_HEADS = 8
DH = 64
GRID_W = 64
ROWS = T // GRID_W
WIN_H = 8
WIN_W = 16
KWIN = WIN_H * GRID_W
N_RPB_R = 2 * WIN_H - 1
N_RPB_C = 2 * WIN_W - 1
N_REC_BLOCKS = 16
REC_BLOCK = 64
CG = 128
N_CG = D_REC // CG
LRU_C = 8.0
EPS = 1e-6
N_DEV = 8
N_CHIPS = 4
LANES = 128

ADAM_LR = 0.001
ADAM_B1 = 0.9
ADAM_B2 = 0.999
ADAM_EPS = 1e-08
ADAM_WD = 0.01
ADAM_STEP = 10

MESH_AXES = ("x", "y", "c")
VMEM_LIMIT = 56 * 1024 * 1024

TILE = 512
DZ_ARRAYS = ((0, 3, 1), (3, 4, 2), (7, 4, 2))
N_DZ_TILES = D_IN // TILE


def _params(**kw):
    return pltpu.CompilerParams(vmem_limit_bytes=VMEM_LIMIT, **kw)


HG = 4
HQ = HG * GRID_W
HC = HG * DH


def _att_tables():
    rq = np.arange(GRID_W)
    kc = np.arange(KWIN) % GRID_W
    win_start = np.clip(rq - WIN_W // 2, 0, GRID_W - WIN_W)
    valid = (kc[None, :] >= win_start[:, None]) & (kc[None, :] < win_start[:, None] + WIN_W)
    same_head = (np.arange(HQ)[:, None] // GRID_W) == (np.arange(HC)[None, :] // DH)
    return valid.astype(np.float32), same_head.astype(np.float32)


def _pair_mask():
    half = np.arange(2 * DH) // DH
    return (half[:, None] == half[None, :]).astype(np.float32)


def _dup_table():
    return np.concatenate([np.eye(REC_BLOCK, dtype=np.float32)] * 2, axis=1)


def _sigmoid(x):
    return 0.5 * jnp.tanh(0.5 * x) + 0.5


def _softplus(x):
    return jnp.maximum(x, 0.0) + jnp.log(1.0 + jnp.exp(-jnp.abs(x)))


def _one_minus_square(log_a, a):
    x = 2.0 * log_a
    series = -x * (1.0 + x * (0.5 + x * (1.0 / 6.0)))
    return jnp.where(x > -0.02, series, 1.0 - a * a)


_GELU_C = math.sqrt(2.0 / math.pi)


def _gelu_and_grad(x):
    x2 = x * x
    inner = _GELU_C * (x + 0.044715 * x * x2)
    t = jnp.tanh(inner)
    g = 0.5 * x * (1.0 + t)
    dg = 0.5 * (1.0 + t) + 0.5 * x * (1.0 - t * t) * _GELU_C * (1.0 + 3.0 * 0.044715 * x2)
    return g, dg


def _dot(a, b):
    return jnp.dot(a, b, preferred_element_type=F32)


def _dot_nt(a, b):
    return lax.dot_general(a, b, (((1,), (1,)), ((), ())), preferred_element_type=F32)


def _dot_tn(a, b):
    return lax.dot_general(a, b, (((0,), (0,)), ((), ())), preferred_element_type=F32)


def _dot_exact(a, b):
    return jnp.dot(a, b, precision=lax.Precision.HIGHEST, preferred_element_type=F32)


def _shift_rows(x, s):
    n = x.shape[0]
    rows = lax.broadcasted_iota(jnp.int32, x.shape, 0)
    y = pltpu.roll(x, s % n, 0)
    if s > 0:
        return jnp.where(rows >= s, y, 0.0)
    return jnp.where(rows < n + s, y, 0.0)


def _rms_bwd(dh, xh, r, g):
    dxh = dh * g
    return r * (dxh - xh * jnp.mean(dxh * xh, axis=-1, keepdims=True))


def _matmul(a, b, mode, out_dtype, name, tm=512, tn=1024, tk=2048):
    if mode == "nn":
        (m, k), (k2, n) = a.shape, b.shape
    elif mode == "nt":
        (m, k), (n, k2) = a.shape, b.shape
    else:
        (k, m), (k2, n) = a.shape, b.shape
    assert k == k2
    tm, tn, tk = min(tm, m), min(tn, n), min(tk, k)
    assert m % tm == 0 and n % tn == 0 and k % tk == 0
    nk = k // tk
    dot = {"nn": _dot, "nt": _dot_nt, "tn": _dot_tn}[mode]

    def body(a_ref, b_ref, o_ref, acc):
        kk = pl.program_id(2)
        part = dot(a_ref[...].astype(BF16), b_ref[...].astype(BF16))
        if nk == 1:
            o_ref[...] = part.astype(out_dtype)
            return

        @pl.when(kk == 0)
        def _():
            acc[...] = part

        @pl.when(kk > 0)
        def _():
            acc[...] += part

        @pl.when(kk == nk - 1)
        def _():
            o_ref[...] = acc[...].astype(out_dtype)

    if mode == "tn":
        a_spec = pl.BlockSpec((tk, tm), lambda i, j, kk: (kk, i))
    else:
        a_spec = pl.BlockSpec((tm, tk), lambda i, j, kk: (i, kk))
    if mode == "nt":
        b_spec = pl.BlockSpec((tn, tk), lambda i, j, kk: (j, kk))
    else:
        b_spec = pl.BlockSpec((tk, tn), lambda i, j, kk: (kk, j))
    return pl.pallas_call(
        body, name=name,
        out_shape=jax.ShapeDtypeStruct((m, n), out_dtype),
        grid=(m // tm, n // tn, nk),
        in_specs=[a_spec, b_spec],
        out_specs=pl.BlockSpec((tm, tn), lambda i, j, kk: (i, j)),
        scratch_shapes=[pltpu.VMEM((tm, tn) if nk > 1 else (8, LANES), F32)],
        compiler_params=_params(dimension_semantics=("parallel", "parallel", "arbitrary")),
    )(a, b)


def _in_proj(x, g1, w_in_t, b_in):
    tm = 512

    def body(x_ref, g_ref, w_hbm, b_ref, qkv_ref, uy_ref, gg_ref, h_ref, w):
        @pl.when(pl.program_id(0) == 0)
        def _():
            pltpu.sync_copy(w_hbm, w)

        xv = x_ref[...]
        r = lax.rsqrt(jnp.mean(xv * xv, axis=-1, keepdims=True) + EPS)
        h = ((xv * r) * g_ref[...]).astype(BF16)
        h_ref[...] = h
        row0 = 0
        for ref in (qkv_ref, uy_ref, gg_ref):
            for c0 in range(0, ref.shape[1], TILE):
                z = _dot_nt(h, w[row0:row0 + TILE, :]) + b_ref[:, row0:row0 + TILE]
                ref[:, c0:c0 + TILE] = z.astype(ref.dtype)
                row0 += TILE

    tok = lambda width: pl.BlockSpec((tm, width), lambda i: (i, 0))
    return pl.pallas_call(
        body, name="in_proj",
        out_shape=(jax.ShapeDtypeStruct((T, 3 * D_ATT), BF16),
                   jax.ShapeDtypeStruct((T, 2 * D_REC), F32),
                   jax.ShapeDtypeStruct((T, 2 * D), F32),
                   jax.ShapeDtypeStruct((T, D), BF16)),
        grid=(T // tm,),
        in_specs=[tok(D), pl.BlockSpec((1, D), lambda i: (0, 0)), pl.BlockSpec(memory_space=pl.ANY),
                  pl.BlockSpec((1, D_IN), lambda i: (0, 0))],
        out_specs=(tok(3 * D_ATT), tok(2 * D_REC), tok(2 * D), tok(D)),
        scratch_shapes=[pltpu.VMEM((D_IN, D), BF16)],
        compiler_params=_params(dimension_semantics=("arbitrary",)),
    )(x, g1, w_in_t, b_in)


def _dz_specs(rows, tile_of, row_of):
    def spec(off, n, per_plane):
        def index(*ids):
            t = jnp.clip(tile_of(*ids) - off, 0, n - 1)
            return (t // per_plane, row_of(*ids), t % per_plane)
        return pl.BlockSpec((1, rows, TILE), index)
    return [spec(off, n, per) for off, n, per in DZ_ARRAYS]


def _dh_norm1_bwd(dz, w_in_t, x, g1, dx1):
    tm = 512

    def body(dqkv_ref, duy_ref, dgg_ref, w_hbm, x_ref, g_ref, dx1_ref, gx_ref, dg_ref, w):
        @pl.when(pl.program_id(0) == 0)
        def _():
            pltpu.sync_copy(w_hbm, w)
            dg_ref[...] = jnp.zeros_like(dg_ref)

        dh, row0 = None, 0
        for ref in (dqkv_ref, duy_ref, dgg_ref):
            for plane in range(ref.shape[0]):
                cols = ref.shape[2]
                part = _dot(ref[plane], w[row0:row0 + cols, :])
                dh = part if dh is None else dh + part
                row0 += cols
        xv = x_ref[...]
        r = lax.rsqrt(jnp.mean(xv * xv, axis=-1, keepdims=True) + EPS)
        xh = xv * r
        dg_ref[...] += jnp.sum(dh * xh, axis=0, keepdims=True)
        gx_ref[...] = dx1_ref[...] + _rms_bwd(dh, xh, r, g_ref[...])

    tok = pl.BlockSpec((tm, D), lambda i: (i, 0))
    vec = pl.BlockSpec((1, D), lambda i: (0, 0))
    planes = lambda a: pl.BlockSpec((a.shape[0], tm, a.shape[2]), lambda i: (0, i, 0))
    return pl.pallas_call(
        body, name="dh_norm1_bwd",
        out_shape=(jax.ShapeDtypeStruct((T, D), F32), jax.ShapeDtypeStruct((1, D), F32)),
        grid=(T // tm,),
        in_specs=[planes(a) for a in dz] + [pl.BlockSpec(memory_space=pl.ANY), tok, vec, tok],
        out_specs=(tok, vec),
        scratch_shapes=[pltpu.VMEM((D_IN, D), BF16)],
        compiler_params=_params(dimension_semantics=("arbitrary",)),
    )(*dz, w_in_t, x, g1, dx1)


def _grad_w_in(dz, h):
    def body(*refs):
        seg_refs = refs[:3]
        h_ref, gw_ref, gb_ref = refs[3:]
        j = pl.program_id(0)

        for s, (off, n, _) in enumerate(DZ_ARRAYS):
            @pl.when((j >= off) & (j < off + n))
            def _(s=s):
                a = seg_refs[s][0]
                gw_ref[...] = _dot_tn(a, h_ref[...]).astype(BF16)
                gb_ref[...] = jnp.sum(a.astype(F32), axis=0, keepdims=True)

    return pl.pallas_call(
        body, name="grad_w_in",
        out_shape=(jax.ShapeDtypeStruct((D_IN, D), BF16), jax.ShapeDtypeStruct((1, D_IN), F32)),
        grid=(N_DZ_TILES,),
        in_specs=_dz_specs(T, lambda j: j, lambda j: 0) + [pl.BlockSpec((T, D), lambda j: (0, 0))],
        out_specs=(pl.BlockSpec((TILE, D), lambda j: (j, 0)), pl.BlockSpec((1, TILE), lambda j: (0, j))),
        compiler_params=_params(dimension_semantics=("parallel",)),
    )(*dz, h)


def _rpb_rows(rpb):
    padded = jnp.pad(rpb, ((0, 0), (0, 0), (0, GRID_W - N_RPB_C)))
    rows = [padded[:, WIN_H - 1 - oi: 2 * WIN_H - 1 - oi].reshape(N_HEADS // HG, HG, KWIN)
            for oi in range(WIN_H)]
    return jnp.stack(rows, axis=0)


SKEW = KWIN - (WIN_W - 1)


MASKED = -1e30


def _bias_tiles(rows_ref, valid, bias_s):
    for oi in range(WIN_H):
        for hh in range(HG):
            row = jnp.broadcast_to(rows_ref[oi, 0, hh:hh + 1, :], (GRID_W, KWIN))
            tile = pltpu.roll(row, SKEW, 1, stride=1, stride_axis=0)
            bias_s[oi, hh * GRID_W:(hh + 1) * GRID_W, :] = jnp.where(valid, tile, MASKED)


def _bias_tile_grads(gb_s, flip, out_ref):
    for oi in range(WIN_H):
        for hh in range(HG):
            g = _dot_exact(flip, gb_s[oi, hh * GRID_W:(hh + 1) * GRID_W, :])
            back = pltpu.roll(g, KWIN - (GRID_W - WIN_W), 1, stride=1, stride_axis=0)
            out_ref[0, oi, hh:hh + 1, :] = jnp.sum(back, axis=0, keepdims=True)


def _rpb_fold(row_grads):
    g = row_grads.transpose(1, 0, 2, 3).reshape(WIN_H, N_HEADS, WIN_H, GRID_W)
    g = g.transpose(0, 2, 1, 3)

    def body(g_ref, o_ref):
        for dr in range(N_RPB_R):
            terms = [g_ref[oi, i] for oi in range(WIN_H) for i in range(WIN_H) if i - oi + WIN_H - 1 == dr]
            acc = terms[0]
            for term in terms[1:]:
                acc = acc + term
            o_ref[dr] = acc

    out = pl.pallas_call(
        body, name="rpb_fold",
        out_shape=jax.ShapeDtypeStruct((N_RPB_R, N_HEADS, GRID_W), F32),
    )(g)
    return out.transpose(1, 0, 2)[:, :, :N_RPB_C]


ATT_GROUPS = N_HEADS // HG
ATT_UNROLL = 8


def _stacked(rows64, same_head):
    return jnp.where(same_head, jnp.concatenate([rows64] * HG, axis=0), jnp.zeros((), BF16))


def _own_heads(stacked):
    head = lax.broadcasted_iota(jnp.int32, (GRID_W, HC), 1) // DH
    out = stacked[:GRID_W]
    for h in range(1, HG):
        out = jnp.where(head == h, stacked[h * GRID_W:(h + 1) * GRID_W], out)
    return out


def _att_scores(q_ref, k_ref, bias_ref, same_head, r):
    rs = jnp.clip(r - WIN_H // 2, 0, ROWS - WIN_H)
    oi = r - rs
    q0 = pl.multiple_of(r * GRID_W, GRID_W)
    k0 = pl.multiple_of(rs * GRID_W, GRID_W)
    q2 = _stacked(q_ref[pl.ds(q0, GRID_W), :] * (DH ** -0.5), same_head)
    kw = k_ref[pl.ds(k0, KWIN), :]
    s = _dot_nt(q2, kw) + bias_ref[oi]
    e = jnp.exp(s - jnp.max(s, axis=-1, keepdims=True))
    return e, 1.0 / jnp.sum(e, axis=-1, keepdims=True), q2, kw, q0, k0, oi


def _att_specs():
    col = lambda off: pl.BlockSpec((T, HC), lambda g: (0, g + off * ATT_GROUPS))
    tables = [pl.BlockSpec((WIN_H, 1, HG, KWIN), lambda g: (0, g, 0, 0)),
              pl.BlockSpec((GRID_W, KWIN), lambda g: (0, 0)),
              pl.BlockSpec((HQ, HC), lambda g: (0, 0))]
    return col, tables, pltpu.VMEM((WIN_H, HQ, KWIN), F32)


def _att_fwd(qkv, bias_rows):
    valid_np, same_head_np = _att_tables()

    def body(q_ref, k_ref, v_ref, rows_ref, valid_ref, head_ref, o_ref, bias_s):
        same_head = head_ref[...] > 0.5
        _bias_tiles(rows_ref, valid_ref[...] > 0.5, bias_s)

        def row(r, carry):
            e, rl, _, _, q0, k0, _ = _att_scores(q_ref, k_ref, bias_s, same_head, r)
            o2 = _dot((e * rl).astype(BF16), v_ref[pl.ds(k0, KWIN), :])
            o_ref[pl.ds(q0, GRID_W), :] = _own_heads(o2).astype(BF16)
            return carry

        lax.fori_loop(0, ROWS, row, 0, unroll=ATT_UNROLL)

    col, tables, tiles = _att_specs()
    return pl.pallas_call(
        body, name="att_fwd",
        out_shape=jax.ShapeDtypeStruct((T, D_ATT), BF16),
        grid=(ATT_GROUPS,),
        in_specs=[col(0), col(1), col(2)] + tables,
        out_specs=col(0),
        scratch_shapes=[tiles],
        compiler_params=_params(dimension_semantics=("parallel",)),
    )(qkv, qkv, qkv, bias_rows, jnp.asarray(valid_np), jnp.asarray(same_head_np))


def _att_bwd(qkv, bias_rows, datt, after):
    valid_np, same_head_np = _att_tables()

    def body(q_ref, k_ref, v_ref, do_ref, rows_ref, valid_ref, head_ref, flip_ref,
             dqkv_ref, grows_ref, dk_acc, dv_acc, bias_s, gb_s):
        same_head = head_ref[...] > 0.5
        dk_acc[...] = jnp.zeros_like(dk_acc)
        dv_acc[...] = jnp.zeros_like(dv_acc)
        gb_s[...] = jnp.zeros_like(gb_s)
        _bias_tiles(rows_ref, valid_ref[...] > 0.5, bias_s)

        def row(r, carry):
            e, rl, q2, kw, q0, k0, oi = _att_scores(q_ref, k_ref, bias_s, same_head, r)
            do2 = _stacked(do_ref[pl.ds(q0, GRID_W), :], same_head)
            vw = v_ref[pl.ds(k0, KWIN), :]
            p = e * rl
            dp = _dot_nt(do2, vw)
            ds = p * (dp - jnp.sum(dp * p, axis=-1, keepdims=True))
            p16 = p.astype(BF16)
            ds16 = ds.astype(BF16)
            dv_acc[pl.ds(k0, KWIN), :] += _dot_tn(p16, do2)
            dk_acc[pl.ds(k0, KWIN), :] += _dot_tn(ds16, q2)
            dq2 = _dot(ds16, kw) * (DH ** -0.5)
            dqkv_ref[0, pl.ds(q0, GRID_W), :] = _own_heads(dq2).astype(BF16)
            gb_s[oi] += ds
            return carry

        lax.fori_loop(0, ROWS, row, 0, unroll=ATT_UNROLL)
        dqkv_ref[1] = dk_acc[...].astype(BF16)
        dqkv_ref[2] = dv_acc[...].astype(BF16)
        _bias_tile_grads(gb_s, flip_ref[...], grows_ref)

    col, tables, tiles = _att_specs()
    return pl.pallas_call(
        body, name="att_bwd",
        out_shape=(jax.ShapeDtypeStruct((3, T, D_ATT), BF16),
                   jax.ShapeDtypeStruct((ATT_GROUPS, WIN_H, HG, KWIN), F32)),
        grid=(ATT_GROUPS,),
        in_specs=[col(0), col(1), col(2), col(0)] + tables + [pl.BlockSpec((GRID_W, GRID_W), lambda g: (0, 0))],
        out_specs=(pl.BlockSpec((3, T, HC), lambda g: (0, 0, g)),
                   pl.BlockSpec((1, WIN_H, HG, KWIN), lambda g: (g, 0, 0, 0))),
        scratch_shapes=[pltpu.VMEM((T, HC), F32), pltpu.VMEM((T, HC), F32), tiles, tiles],
        compiler_params=_params(dimension_semantics=("parallel",)),
    )(qkv, qkv, qkv, datt, bias_rows, jnp.asarray(valid_np) + after, jnp.asarray(same_head_np),
      jnp.asarray(np.eye(GRID_W, dtype=np.float32)[::-1].copy()))


def _conv_taps(up):
    return (_shift_rows(up, 2), _shift_rows(up, 1), up, _shift_rows(up, -1))


def _pair_block_diag(w_pair, dup, same_half):
    return jnp.where(same_half, _dot(w_pair.astype(BF16), dup), 0.0).astype(BF16)


def _gates(u, u16, wa, ba, wi, bi, lam):
    r = _sigmoid(_dot(u16, wa) + ba)
    ig = _sigmoid(_dot(u16, wi) + bi)
    sp = _softplus(-lam)
    log_a = (-LRU_C) * r * sp
    a = jnp.exp(log_a)
    mult2 = jnp.maximum(_one_minus_square(log_a, a), 0.0)
    return r, ig, sp, a, jnp.sqrt(mult2), mult2


SCAN_BLOCKS = 8


def _scans(jobs):
    c = jobs[0][0].shape[1]
    nblk = T // 8
    rows = lax.broadcasted_iota(jnp.int32, (8, c), 0)

    def block(a, b, reverse):
        for s in (1, 2, 4):
            if reverse:
                keep = rows < 8 - s
                a_s = jnp.where(keep, pltpu.roll(a, 8 - s, 0), 1.0)
                b_s = jnp.where(keep, pltpu.roll(b, 8 - s, 0), 0.0)
            else:
                keep = rows >= s
                a_s = jnp.where(keep, pltpu.roll(a, s, 0), 1.0)
                b_s = jnp.where(keep, pltpu.roll(b, s, 0), 0.0)
            b = a * b_s + b
            a = a * a_s
        return a, b

    def step(i, carry):
        out = []
        for (a_ref, b_ref, h_ref, reverse), h_prev in zip(jobs, carry):
            for u in range(SCAN_BLOCKS):
                blk = i * SCAN_BLOCKS + u
                if reverse:
                    blk = nblk - 1 - blk
                t0 = pl.multiple_of(blk * 8, 8)
                a, b = block(a_ref[pl.ds(t0, 8), :], b_ref[pl.ds(t0, 8), :], reverse)
                h = a * h_prev + b
                h_ref[pl.ds(t0, 8), :] = h
                h_prev = jnp.broadcast_to(h[0:1] if reverse else h[7:8], (8, c))
            out.append(h_prev)
        return tuple(out)

    lax.fori_loop(0, nblk // SCAN_BLOCKS, step, tuple(jnp.zeros((8, c), F32) for _ in jobs))


def _rec_specs():
    tok = lambda off: pl.BlockSpec((T, CG), lambda g: (0, g + off))
    per_ch = lambda rows: pl.BlockSpec((rows, CG), lambda g: (0, g))
    wspec = pl.BlockSpec((2, 1, CG, REC_BLOCK), lambda g: (0, g, 0, 0))
    const = lambda shape: pl.BlockSpec(shape, lambda g: (0, 0))
    return tok, per_ch, wspec, const


def _rec_fwd(uy, conv_w, conv_b, w_a, b_a, w_i, b_i, lam):
    tok, per_ch, wspec, const = _rec_specs()

    def body(up_ref, yb_ref, cw_ref, cb_ref, wa_ref, ba_ref, wi_ref, bi_ref, lam_ref, dup_ref, half_ref,
             hf_ref, hb_ref, yrec_ref, am_ref, bx_f, bx_b):
        dup = dup_ref[...]
        same_half = half_ref[...] > 0.5
        taps = _conv_taps(up_ref[...])
        u = cb_ref[...]
        for j in range(4):
            u = u + taps[j] * cw_ref[j:j + 1, :]
        u16 = u.astype(BF16)
        for d, bx_s in enumerate((bx_f, bx_b)):
            wa = _pair_block_diag(wa_ref[d, 0], dup, same_half)
            wi = _pair_block_diag(wi_ref[d, 0], dup, same_half)
            _, ig, _, a, mult, _ = _gates(u, u16, wa, ba_ref[d:d + 1, :], wi, bi_ref[d:d + 1, :],
                                       lam_ref[d:d + 1, :])
            am_ref[2 * d] = a
            am_ref[2 * d + 1] = mult
            bx_s[...] = mult * (ig * u)
        _scans([(am_ref.at[0], bx_f, hf_ref, False), (am_ref.at[2], bx_b, hb_ref, True)])
        gelu, _ = _gelu_and_grad(yb_ref[...])
        yrec_ref[...] = ((hf_ref[...] + hb_ref[...]) * gelu).astype(BF16)

    return pl.pallas_call(
        body, name="rec_fwd",
        out_shape=(jax.ShapeDtypeStruct((T, D_REC), F32), jax.ShapeDtypeStruct((T, D_REC), F32),
                   jax.ShapeDtypeStruct((T, D_REC), BF16), jax.ShapeDtypeStruct((4, T, D_REC), F32)),
        grid=(N_CG,),
        in_specs=[tok(0), tok(N_CG), per_ch(4), per_ch(1), wspec, per_ch(2), wspec, per_ch(2), per_ch(2),
                  const((REC_BLOCK, CG)), const((CG, CG))],
        out_specs=(tok(0), tok(0), tok(0), pl.BlockSpec((4, T, CG), lambda g: (0, 0, g))),
        scratch_shapes=[pltpu.VMEM((T, CG), F32)] * 2,
        compiler_params=_params(dimension_semantics=("parallel",)),
    )(uy, uy, conv_w, conv_b, w_a, b_a, w_i, b_i, lam,
      jnp.asarray(_dup_table(), BF16), jnp.asarray(_pair_mask()))


def _rec_bwd(uy, hf, hb, am, dyrec, conv_w, conv_b, w_a, b_a, w_i, b_i, lam):
    tok, per_ch, wspec, const = _rec_specs()

    def body(up_ref, yb_ref, hf_ref, hb_ref, am_ref, dy_ref, cw_ref, cb_ref, wa_ref, ba_ref, wi_ref, bi_ref,
             lam_ref, dup_ref, dupt_ref, half_ref,
             duy_ref, dcw_ref, dcb_ref, dwa_ref, dba_ref, dwi_ref, dbi_ref, dlam_ref,
             a_s0, a_s1, dh_s, g_s0, g_s1):
        dup = dup_ref[...]
        dup_t = dupt_ref[...]
        same_half = half_ref[...] > 0.5
        taps = _conv_taps(up_ref[...])
        u = cb_ref[...]
        for j in range(4):
            u = u + taps[j] * cw_ref[j:j + 1, :]
        u16 = u.astype(BF16)
        gelu, dgelu = _gelu_and_grad(yb_ref[...])
        dy = dy_ref[...]
        duy_ref[1] = (dy * (hf_ref[...] + hb_ref[...]) * dgelu).astype(BF16)
        dh_s[...] = dy * gelu
        a_s0[...] = _shift_rows(am_ref[0], -1)
        a_s1[...] = _shift_rows(am_ref[2], 1)
        _scans([(a_s0, dh_s, g_s0, True), (a_s1, dh_s, g_s1, False)])
        du = jnp.zeros((T, CG), F32)
        for d, g_s in enumerate((g_s0, g_s1)):
            reverse = d == 1
            wa = _pair_block_diag(wa_ref[d, 0], dup, same_half)
            wi = _pair_block_diag(wi_ref[d, 0], dup, same_half)
            lam_d = lam_ref[d:d + 1, :]
            r = _sigmoid(_dot(u16, wa) + ba_ref[d:d + 1, :])
            ig = _sigmoid(_dot(u16, wi) + bi_ref[d:d + 1, :])
            sp = _softplus(-lam_d)
            a, mult = am_ref[2 * d], am_ref[2 * d + 1]
            mult2 = mult * mult
            g = g_s[...]
            h_prev = _shift_rows(hb_ref[...], -1) if reverse else _shift_rows(hf_ref[...], 1)
            da = g * h_prev
            dmult = g * (ig * u)
            dig = g * mult * u
            du = du + g * mult * ig
            dmult_dlog = jnp.where(mult2 > 0.0, -(a * a) * lax.rsqrt(mult2), 0.0)
            dlog_a = da * a + dmult * dmult_dlog
            dr = dlog_a * ((-LRU_C) * sp)
            dsp = jnp.sum(dlog_a * ((-LRU_C) * r), axis=0, keepdims=True)
            dlam_ref[d:d + 1, :] = dsp * (-_sigmoid(-lam_d))
            dga = dr * r * (1.0 - r)
            dgi = dig * ig * (1.0 - ig)
            dga16 = dga.astype(BF16)
            dgi16 = dgi.astype(BF16)
            du = du + _dot_nt(dga16, wa) + _dot_nt(dgi16, wi)
            dwa_ref[d, 0] = _dot_exact(jnp.where(same_half, _dot_tn(u16, dga16), 0.0), dup_t)
            dwi_ref[d, 0] = _dot_exact(jnp.where(same_half, _dot_tn(u16, dgi16), 0.0), dup_t)
            dba_ref[d:d + 1, :] = jnp.sum(dga, axis=0, keepdims=True)
            dbi_ref[d:d + 1, :] = jnp.sum(dgi, axis=0, keepdims=True)
        dcb_ref[...] = jnp.sum(du, axis=0, keepdims=True)
        for j in range(4):
            dcw_ref[j:j + 1, :] = jnp.sum(du * taps[j], axis=0, keepdims=True)
        dup_in = (_shift_rows(du, -2) * cw_ref[0:1, :] + _shift_rows(du, -1) * cw_ref[1:2, :]
                  + du * cw_ref[2:3, :] + _shift_rows(du, 1) * cw_ref[3:4, :])
        duy_ref[0] = dup_in.astype(BF16)

    wshape = jax.ShapeDtypeStruct((2, N_CG, CG, REC_BLOCK), F32)
    vec = lambda rows: jax.ShapeDtypeStruct((rows, D_REC), F32)
    dup_np = _dup_table()
    return pl.pallas_call(
        body, name="rec_bwd",
        out_shape=(jax.ShapeDtypeStruct((2, T, D_REC), BF16),
                   vec(4), vec(1), wshape, vec(2), wshape, vec(2), vec(2)),
        grid=(N_CG,),
        in_specs=[tok(0), tok(N_CG), tok(0), tok(0), pl.BlockSpec((4, T, CG), lambda g: (0, 0, g)), tok(0),
                  per_ch(4), per_ch(1), wspec, per_ch(2), wspec, per_ch(2), per_ch(2),
                  const((REC_BLOCK, CG)), const((CG, REC_BLOCK)), const((CG, CG))],
        out_specs=(pl.BlockSpec((2, T, CG), lambda g: (0, 0, g)),
                   per_ch(4), per_ch(1), wspec, per_ch(2), wspec, per_ch(2), per_ch(2)),
        scratch_shapes=[pltpu.VMEM((T, CG), F32)] * 5,
        compiler_params=_params(dimension_semantics=("parallel",)),
    )(uy, uy, hf, hb, am, dyrec, conv_w, conv_b, w_a, b_a, w_i, b_i, lam,
      jnp.asarray(dup_np, BF16), jnp.asarray(dup_np.T.copy()), jnp.asarray(_pair_mask()))


TM_MIX = 256


def _mix_specs():
    tok = lambda width, blk=0: pl.BlockSpec((TM_MIX, width), lambda i: (i, blk))
    full = lambda shape: pl.BlockSpec(shape, lambda i: (0, 0))
    return tok, full


def _mix_fwd(x, att, yrec, gg, w_att_o_t, w_rec_o, w_out):
    tok, full = _mix_specs()

    def body(x_ref, att_ref, yr_ref, ga_ref, gr_ref, wao_ref, wro_ref, wo_ref, x1_ref, mixed_ref):
        y_att = _dot_nt(att_ref[...], wao_ref[...])
        y_rec = _dot(yr_ref[...], wro_ref[...])
        mixed = (_sigmoid(ga_ref[...]) * y_att + _sigmoid(gr_ref[...]) * y_rec).astype(BF16)
        mixed_ref[...] = mixed
        x1_ref[...] = x_ref[...] + _dot(mixed, wo_ref[...])

    return pl.pallas_call(
        body, name="mix_fwd",
        out_shape=(jax.ShapeDtypeStruct((T, D), F32), jax.ShapeDtypeStruct((T, D), BF16)),
        grid=(T // TM_MIX,),
        in_specs=[tok(D), tok(D_ATT), tok(D_REC), tok(D, 0), tok(D, 1),
                  full((D, D_ATT)), full((D_REC, D)), full((D, D))],
        out_specs=(tok(D), tok(D)),
        compiler_params=_params(dimension_semantics=("parallel",)),
    )(x, att, yrec, gg, gg, w_att_o_t, w_rec_o, w_out)


def _mix_bwd(dx1, att, yrec, gg, w_att_o_t, w_rec_o, w_out, after):
    tok, full = _mix_specs()

    def body(dx_ref, att_ref, yr_ref, ga_ref, gr_ref, wao_ref, wro_ref, wo_ref, after_ref,
             dgg_ref, dya_ref, dyr_ref, datt_ref, dyrp_ref):
        dmixed = _dot_nt(dx_ref[...].astype(BF16), wo_ref[...])
        y_att = _dot_nt(att_ref[...], wao_ref[...])
        y_rec = _dot(yr_ref[...], wro_ref[...])
        sa = _sigmoid(ga_ref[...])
        sr = _sigmoid(gr_ref[...])
        dgg_ref[0] = (dmixed * y_att * sa * (1.0 - sa)).astype(BF16)
        dgg_ref[1] = (dmixed * y_rec * sr * (1.0 - sr)).astype(BF16)
        dya = (dmixed * sa).astype(BF16)
        dyr = (dmixed * sr).astype(BF16)
        dya_ref[...] = dya
        dyr_ref[...] = dyr
        datt_ref[...] = _dot(dya, wao_ref[...]).astype(BF16)
        dyrp_ref[...] = _dot_nt(dyr, wro_ref[...])

    return pl.pallas_call(
        body, name="mix_bwd",
        out_shape=(jax.ShapeDtypeStruct((2, T, D), BF16),
                   jax.ShapeDtypeStruct((T, D), BF16), jax.ShapeDtypeStruct((T, D), BF16),
                   jax.ShapeDtypeStruct((T, D_ATT), BF16), jax.ShapeDtypeStruct((T, D_REC), F32)),
        grid=(T // TM_MIX,),
        in_specs=[tok(D), tok(D_ATT), tok(D_REC), tok(D, 0), tok(D, 1),
                  full((D, D_ATT)), full((D_REC, D)), full((D, D)), pl.BlockSpec(memory_space=pl.ANY)],
        out_specs=(pl.BlockSpec((2, TM_MIX, D), lambda i: (0, i, 0)),
                   tok(D), tok(D), tok(D_ATT), tok(D_REC)),
        compiler_params=_params(dimension_semantics=("parallel",)),
    )(dx1, att, yrec, gg, gg, w_att_o_t, w_rec_o, w_out, after)


TM_FFN = 256
FF_CHUNK = 1024


def _ffn_loss(x1, target, g2, gf, w_ff1_t, w_ff2):
    n_chunks = D_FF // FF_CHUNK

    def body(x1_ref, tg_ref, g2_ref, gf_ref, w1_hbm, w2_hbm,
             loss_ref, dx1_ref, h2_ref, act_ref, dpre_ref, dx2_ref, dg2_ref, dgf_ref,
             w1, w2, relu_s):
        i = pl.program_id(0)

        @pl.when(i == 0)
        def _():
            pltpu.sync_copy(w1_hbm, w1)
            pltpu.sync_copy(w2_hbm, w2)
            loss_ref[...] = jnp.zeros_like(loss_ref)
            dg2_ref[...] = jnp.zeros_like(dg2_ref)
            dgf_ref[...] = jnp.zeros_like(dgf_ref)

        x1v = x1_ref[...]
        r2 = lax.rsqrt(jnp.mean(x1v * x1v, axis=-1, keepdims=True) + EPS)
        xh2 = x1v * r2
        h2 = (xh2 * g2_ref[...]).astype(BF16)
        h2_ref[...] = h2
        x2 = x1v
        for c in range(n_chunks):
            ff = slice(c * FF_CHUNK, (c + 1) * FF_CHUNK)
            rl = jnp.maximum(_dot_nt(h2, w1[ff, :]), 0.0)
            relu_s[:, ff] = rl
            act = (rl * rl).astype(BF16)
            act_ref[:, ff] = act
            x2 = x2 + _dot(act, w2[ff, :])
        r3 = lax.rsqrt(jnp.mean(x2 * x2, axis=-1, keepdims=True) + EPS)
        xh3 = x2 * r3
        err = xh3 * gf_ref[...] - tg_ref[...]
        loss_ref[...] += 0.5 * jnp.sum(jnp.mean(err * err, axis=-1, keepdims=True))
        dy = err * (1.0 / D)
        dgf_ref[...] += jnp.sum(dy * xh3, axis=0, keepdims=True)
        dx2 = _rms_bwd(dy, xh3, r3, gf_ref[...])
        dx2_16 = dx2.astype(BF16)
        dx2_ref[...] = dx2_16
        dh2 = jnp.zeros((TM_FFN, D), F32)
        for c in range(n_chunks):
            ff = slice(c * FF_CHUNK, (c + 1) * FF_CHUNK)
            dpre = (_dot_nt(dx2_16, w2[ff, :]) * (2.0 * relu_s[:, ff])).astype(BF16)
            dpre_ref[:, ff] = dpre
            dh2 = dh2 + _dot(dpre, w1[ff, :])
        dg2_ref[...] += jnp.sum(dh2 * xh2, axis=0, keepdims=True)
        dx1_ref[...] = dx2 + _rms_bwd(dh2, xh2, r2, g2_ref[...])

    tok = lambda width: pl.BlockSpec((TM_FFN, width), lambda i: (i, 0))
    vec = pl.BlockSpec((1, D), lambda i: (0, 0))
    hbm = pl.BlockSpec(memory_space=pl.ANY)
    return pl.pallas_call(
        body, name="ffn_loss",
        out_shape=(jax.ShapeDtypeStruct((8, 128), F32), jax.ShapeDtypeStruct((T, D), F32),
                   jax.ShapeDtypeStruct((T, D), BF16), jax.ShapeDtypeStruct((T, D_FF), BF16),
                   jax.ShapeDtypeStruct((T, D_FF), BF16), jax.ShapeDtypeStruct((T, D), BF16),
                   jax.ShapeDtypeStruct((1, D), F32), jax.ShapeDtypeStruct((1, D), F32)),
        grid=(T // TM_FFN,),
        in_specs=[tok(D), tok(D), vec, vec, hbm, hbm],
        out_specs=(pl.BlockSpec((8, 128), lambda i: (0, 0)), tok(D), tok(D), tok(D_FF), tok(D_FF), tok(D),
                   vec, vec),
        scratch_shapes=[pltpu.VMEM((D_FF, D), BF16), pltpu.VMEM((D_FF, D), BF16),
                        pltpu.VMEM((TM_FFN, D_FF), F32)],
        compiler_params=_params(dimension_semantics=("arbitrary",)),
    )(x1, target, g2, gf, w_ff1_t, w_ff2)


def _local_step(x, target, p, late_weights, reduce_first, reduce_early):
    bias = _rpb_rows(p["rpb"])
    pairs = lambda w: w.reshape(2, N_CG, CG, REC_BLOCK)
    w_a, w_i = pairs(p["w_rg_a"]), pairs(p["w_rg_i"])
    rec_params = (p["conv_w"], p["conv_b"], w_a, p["b_rg_a"], w_i, p["b_rg_i"], p["lru_lambda"])

    qkv, uy, gg, h = _in_proj(x, p["ln1_g"], p["w_in_t"], p["b_in"])
    att = _att_fwd(qkv, bias)
    hf, hb, yrec, am = _rec_fwd(uy, *rec_params)
    p = {**p, **late_weights(yrec, 0)}
    x1, mixed = _mix_fwd(x, att, yrec, gg, p["w_att_o_t"], p["w_rec_o"], p["w_out"])
    p = {**p, **late_weights(x1, 1)}
    loss8, dx1, h2, act, dpre, dx2, g_ln2, g_lnf = _ffn_loss(
        x1, target, p["ln2_g"], p["lnf_g"], p["w_ff1_t"], p["w_ff2"])

    grads = {"ln2_g": g_ln2, "lnf_g": g_lnf,
             "w_ff1_t": _matmul(dpre, h2, "tn", BF16, "g_w_ff1"),
             "w_ff2": _matmul(act, dx2, "tn", BF16, "g_w_ff2")}
    dgg, dya, dyr, datt, dyrp = _mix_bwd(dx1, att, yrec, gg, p["w_att_o_t"], p["w_rec_o"], p["w_out"],
                                         reduce_first(grads, None))
    lam_after = rec_params[-1] + reduce_first(None, dgg)[0, 0]
    duy, g_cw, g_cb, g_wa, g_ba, g_wi, g_bi, g_lam = _rec_bwd(uy, hf, hb, am, dyrp, *rec_params[:-1], lam_after)
    blocks = lambda g: g.reshape(2, N_REC_BLOCKS, REC_BLOCK, REC_BLOCK)
    grads.update({
        "w_att_o_t": _matmul(dya, att, "tn", BF16, "g_w_att_o"),
        "conv_w": g_cw, "conv_b": g_cb, "w_rg_a": blocks(g_wa), "b_rg_a": g_ba,
        "w_rg_i": blocks(g_wi), "b_rg_i": g_bi, "lru_lambda": g_lam,
        "w_rec_o": _matmul(yrec, dyr, "tn", BF16, "g_w_rec_o"),
        "w_out": _matmul(mixed, dx1, "tn", BF16, "g_w_out"),
    })
    dqkv, gbias = _att_bwd(qkv, bias, datt, reduce_early(grads))
    dz = (dqkv, duy, dgg)
    grad_x, g_ln1 = _dh_norm1_bwd(dz, p["w_in_t"], x, p["ln1_g"], dx1)
    g_w_in_t, g_b_in = _grad_w_in(dz, h)
    grads.update(ln1_g=g_ln1, w_in_t=g_w_in_t, b_in=g_b_in, rpb=_rpb_fold(gbias))
    return loss8[0:1, 0:1], grad_x, grads


MESH_ID = pl.DeviceIdType.MESH
ANY = pl.BlockSpec(memory_space=pl.ANY)

CHAN_BLOCK_ROWS = 32
GATE_ROWS = 2 * 2 * N_REC_BLOCKS * REC_BLOCK * REC_BLOCK // (N_DEV * D)
SECTIONS = (("w_in_t", 704, D), ("w_rec_o", 128, D), ("w_out", 128, D), ("w_ff1_t", 512, D),
            ("w_ff2", 512, D), ("chan", CHAN_BLOCK_ROWS, D), ("w_att_o_t", 128, D_ATT),
            ("gates", GATE_ROWS, D))
N_SEC = len(SECTIONS)
N_CHAN_ROWS = 10
CHAN = (("conv_w", 4), ("b_rg_a", 2), ("b_rg_i", 2), ("lru_lambda", 2))


def _position():
    return lax.axis_index("x"), lax.axis_index("y"), lax.axis_index("c")


def _other_chips(x, y):
    return [(1 - x, y), (x, 1 - y), (1 - x, 1 - y)]


PASS_ON_IDS, PAIR_EARLY_ID, PAIR_LATE_ID, PAIR_FIRST_ID = (1, 4), 2, 3, 5


def _pair_handshake(x, y, c):
    barrier = pltpu.get_barrier_semaphore()
    pl.semaphore_signal(barrier, inc=1, device_id=(x, y, 1 - c), device_id_type=MESH_ID)
    pl.semaphore_wait(barrier, 1)


def _block_of(ref, dev, rows):
    return ref.at[pl.ds(pl.multiple_of(dev * rows, 16), rows)]


def _all_gather(shards, name):
    ns = len(shards)

    def body(*refs):
        x_refs, out_refs, done_ref = refs[:ns], refs[ns:2 * ns], refs[2 * ns]
        send_sems, recv_sems, local_sems = refs[2 * ns + 1:]
        done_ref[0, 0] = 0.0
        x, y, c = _position()
        me, sibling = (x, y, c), (x, y, 1 - c)
        x_nbr, y_nbr, diagonal = _other_chips(x, y)
        north = c == 1
        relay_from = (jnp.where(north, x_nbr[0], y_nbr[0]), jnp.where(north, x_nbr[1], y_nbr[1]))
        relay_to = (jnp.where(north, y_nbr[0], x_nbr[0]), jnp.where(north, y_nbr[1], x_nbr[1]))

        def rows(s, px, py, pc):
            return _block_of(out_refs[s], 4 * px + 2 * py + pc, shards[s].shape[0])

        def copy(k, s, block, to, from_shard=False):
            return pltpu.make_async_remote_copy(
                src_ref=x_refs[s] if from_shard else rows(s, *block), dst_ref=rows(s, *block),
                send_sem=send_sems.at[k * ns + s], recv_sem=recv_sems.at[k * ns + s],
                device_id=to, device_id_type=MESH_ID)

        sections = range(ns)
        mine = [pltpu.make_async_copy(x_refs[s], rows(s, *me), local_sems.at[s]) for s in sections]
        sent = [copy(k, s, me, to, True) for k, to in enumerate((sibling, (*x_nbr, c), (*y_nbr, c)))
                for s in sections]
        for cp in mine + sent:
            cp.start()
        for s in sections:
            copy(1, s, (*x_nbr, c), me).wait_recv()
            copy(2, s, (*y_nbr, c), me).wait_recv()
            sent += [copy(3, s, (*relay_from, c), (*relay_to, c)),
                     copy(4, s, (*x_nbr, c), sibling), copy(5, s, (*y_nbr, c), sibling)]
            for cp in sent[-3:]:
                cp.start()
        for s in sections:
            copy(3, s, (*diagonal, c), me).wait_recv()
            sent.append(copy(6, s, (*diagonal, c), sibling))
            sent[-1].start()
        for s in sections:
            copy(0, s, sibling, me).wait_recv()
            for k, chip in ((4, x_nbr), (5, y_nbr), (6, diagonal)):
                copy(k, s, (*chip, 1 - c), me).wait_recv()
        for cp in sent:
            cp.wait_send()
        for cp in mine:
            cp.wait()

    return pl.pallas_call(
        body, name=name,
        out_shape=tuple(jax.ShapeDtypeStruct((N_DEV * s.shape[0], s.shape[1]), s.dtype) for s in shards)
        + (jax.ShapeDtypeStruct((1, 1), F32),),
        in_specs=[ANY] * ns,
        out_specs=(ANY,) * ns + (pl.BlockSpec(memory_space=pltpu.SMEM),),
        scratch_shapes=[pltpu.SemaphoreType.DMA((7 * ns,)), pltpu.SemaphoreType.DMA((7 * ns,)),
                        pltpu.SemaphoreType.DMA((ns,))],
    )(*shards)


HBM = pl.BlockSpec(memory_space=pltpu.HBM)
SEM = pl.BlockSpec(memory_space=pltpu.SEMAPHORE)
EFFECT = pltpu.SideEffectType.DATAFLOW_SIDE_EFFECTING


def _in_hbm(a):
    return pltpu.with_memory_space_constraint(a, pltpu.HBM)


def _first_hop_copies(shards, x_refs, zones, send_sems, recv_sems):
    ns = len(shards)
    x, y, c = _position()
    targets = [(x, y, 1 - c)] + [(cx, cy, c) for cx, cy in _other_chips(x, y)]
    return [pltpu.make_async_remote_copy(
        src_ref=x_refs[s], dst_ref=_block_of(zones[s], 4 * x + 2 * y + c, shards[s].shape[0]),
        send_sem=send_sems.at[k * ns + s], recv_sem=recv_sems.at[k * ns + s],
        device_id=to, device_id_type=MESH_ID)
        for k, to in enumerate(targets) for s in range(ns)]


def _after_all(arrays, name):
    def body(*refs):
        refs[-1][...] = jnp.zeros_like(refs[-1])

    return pl.pallas_call(
        body, name=name,
        out_shape=jax.ShapeDtypeStruct((8, LANES), F32),
        in_specs=[pl.BlockSpec(memory_space=pl.ANY)] * len(arrays),
        out_specs=pl.BlockSpec(memory_space=pltpu.VMEM),
    )(*arrays)


def _own_blocks_placed(shards, after):
    ns = len(shards)
    x, y, c = _position()
    me = jnp.reshape(4 * x + 2 * y + c, (1,)).astype(jnp.int32)
    shards = [*shards[:-1], shards[-1] + after.astype(shards[-1].dtype)]

    def body(me_ref, *refs):
        for s in range(ns):
            refs[ns + s][...] = refs[s][...]

    return pl.pallas_call(
        body, name="own_blocks_placed",
        out_shape=tuple(jax.ShapeDtypeStruct((N_DEV * s.shape[0], s.shape[1]), s.dtype) for s in shards),
        grid_spec=pltpu.PrefetchScalarGridSpec(
            num_scalar_prefetch=1, grid=(1,),
            in_specs=[pl.BlockSpec(s.shape, lambda i, me: (0, 0)) for s in shards],
            out_specs=tuple(pl.BlockSpec(s.shape, lambda i, me: (me[0], 0)) for s in shards)),
        compiler_params=_params(dimension_semantics=("arbitrary",)),
    )(me, *shards)


def _gather_start(shards, after, name):
    ns = len(shards)
    zones = _own_blocks_placed(shards, after)

    def body(*refs):
        for cp in _first_hop_copies(shards, refs[:ns], refs[ns:2 * ns], refs[2 * ns], refs[2 * ns + 1]):
            cp.start()
        refs[-1][...] = jnp.zeros_like(refs[-1])

    out = pl.pallas_call(
        body, name=name,
        out_shape=(pltpu.SemaphoreType.DMA((4 * ns,)), pltpu.SemaphoreType.DMA((4 * ns,)),
                   *[pltpu.HBM(a.shape, a.dtype) for a in (*shards, *zones)],
                   jax.ShapeDtypeStruct((8, LANES), F32)),
        in_specs=[HBM] * (2 * ns),
        out_specs=(SEM, SEM, *[HBM] * (2 * ns), pl.BlockSpec(memory_space=pltpu.VMEM)),
        input_output_aliases={i: 2 + i for i in range(2 * ns)},
        compiler_params=pltpu.CompilerParams(has_side_effects=EFFECT),
    )(*[_in_hbm(a) for a in shards], *[_in_hbm(a) for a in zones])
    return out[0], out[1], out[2:2 + ns], out[2 + ns:2 + 2 * ns], out[-1]


def _gather_wait(send_sems, recv_sems, shards, zones, which, after, name):
    ns = len(shards)

    def body(*refs):
        copies = _first_hop_copies(shards, refs[:ns], refs[ns:2 * ns], refs[2 * ns], refs[2 * ns + 1])
        for i, cp in enumerate(copies):
            if i % ns in which:
                cp.wait_send()
                cp.wait_recv()

    out = pl.pallas_call(
        body, name=name,
        out_shape=tuple(pltpu.HBM(a.shape, a.dtype) for a in (*shards, *zones)),
        in_specs=[HBM] * (2 * ns) + [SEM, SEM, ANY],
        out_specs=(HBM,) * (2 * ns),
        input_output_aliases={i: i for i in range(2 * ns)},
        compiler_params=pltpu.CompilerParams(has_side_effects=EFFECT),
    )(*shards, *zones, send_sems, recv_sems, after)
    return out[:ns], out[ns:]


def _gather_pass_on(rows, zones, barrier_id, name):
    ns = len(zones)

    def body(*refs):
        in_refs, out_refs = refs[:ns], refs[ns:2 * ns]
        send_sems, recv_sems = refs[2 * ns:]
        x, y, c = _position()
        _pair_handshake(x, y, c)
        copies = [pltpu.make_async_remote_copy(
            src_ref=_block_of(in_refs[s], 4 * cx + 2 * cy + c, rows[s]),
            dst_ref=_block_of(out_refs[s], 4 * cx + 2 * cy + c, rows[s]),
            send_sem=send_sems.at[j * ns + s], recv_sem=recv_sems.at[j * ns + s],
            device_id=(x, y, 1 - c), device_id_type=MESH_ID)
            for j, (cx, cy) in enumerate(_other_chips(x, y)) for s in range(ns)]
        for cp in copies:
            cp.start()
        for cp in copies:
            cp.wait_recv()
        for cp in copies:
            cp.wait_send()

    return pl.pallas_call(
        body, name=name,
        out_shape=tuple(jax.ShapeDtypeStruct(z.shape, z.dtype) for z in zones),
        in_specs=[ANY] * ns, out_specs=(ANY,) * ns,
        input_output_aliases={i: i for i in range(ns)},
        scratch_shapes=[pltpu.SemaphoreType.DMA((3 * ns,)), pltpu.SemaphoreType.DMA((3 * ns,))],
        compiler_params=pltpu.CompilerParams(collective_id=barrier_id),
    )(*zones)


def _pair_copies(sections, g_refs, land, send_sems, recv_sems):
    ns = len(sections)
    x, y, c = _position()
    return [pltpu.make_async_remote_copy(
        src_ref=_block_of(g_refs[s], 2 * k + 1 - c, rows), dst_ref=land[s].at[k],
        send_sem=send_sems.at[k * ns + s], recv_sem=recv_sems.at[k * ns + s],
        device_id=(x, y, 1 - c), device_id_type=MESH_ID)
        for k in range(N_CHIPS) for s, (_, rows, _) in enumerate(sections)]


def _pair_exchange_start(sections, grads, barrier_id, name):
    ns = len(sections)

    def body(*refs):
        _pair_handshake(*_position())
        for cp in _pair_copies(sections, refs[:ns], refs[ns:2 * ns], refs[2 * ns], refs[2 * ns + 1]):
            cp.start()
        refs[-1][...] = jnp.zeros_like(refs[-1])

    zones = [lax.empty((N_CHIPS, rows, cols), BF16) for _, rows, cols in sections]
    n = N_CHIPS * ns
    out = pl.pallas_call(
        body, name=name,
        out_shape=(pltpu.SemaphoreType.DMA((n,)), pltpu.SemaphoreType.DMA((n,)),
                   *[pltpu.HBM(a.shape, a.dtype) for a in (*grads, *zones)],
                   jax.ShapeDtypeStruct((8, LANES), F32)),
        in_specs=[HBM] * (2 * ns),
        out_specs=(SEM, SEM, *[HBM] * (2 * ns), pl.BlockSpec(memory_space=pltpu.VMEM)),
        input_output_aliases={i: 2 + i for i in range(2 * ns)},
        compiler_params=pltpu.CompilerParams(has_side_effects=EFFECT, collective_id=barrier_id),
    )(*[_in_hbm(a) for a in grads], *[_in_hbm(a) for a in zones])
    return out[0], out[1], out[2:2 + ns], out[2 + ns:2 + 2 * ns], out[-1]


def _pair_exchange_wait(sections, send_sems, recv_sems, grads, zones, after, name):
    ns = len(sections)

    def body(*refs):
        for cp in _pair_copies(sections, refs[:ns], refs[ns:2 * ns], refs[2 * ns], refs[2 * ns + 1]):
            cp.wait_send()
            cp.wait_recv()

    out = pl.pallas_call(
        body, name=name,
        out_shape=tuple(pltpu.HBM(a.shape, a.dtype) for a in (*grads, *zones)),
        in_specs=[HBM] * (2 * ns) + [SEM, SEM, ANY],
        out_specs=(HBM,) * (2 * ns),
        input_output_aliases={i: i for i in range(2 * ns)},
        compiler_params=pltpu.CompilerParams(has_side_effects=EFFECT),
    )(*grads, *zones, send_sems, recv_sems, after)
    return out[:ns], out[ns:]


def _pair_add(sections, grads, got, core, name):
    ns = len(sections)

    def body(core_ref, *refs):
        g_refs, got_refs, p_refs = refs[:ns], refs[ns:2 * ns], refs[2 * ns:]
        for s in range(ns):
            p_refs[s][0] = (g_refs[s][...].astype(F32) + got_refs[s][0].astype(F32)).astype(BF16)

    slot = [pl.BlockSpec((1, rows, cols), lambda k, c: (k, 0, 0)) for _, rows, cols in sections]
    return pl.pallas_call(
        body, name=name,
        out_shape=tuple(jax.ShapeDtypeStruct((N_CHIPS, rows, cols), BF16) for _, rows, cols in sections),
        grid_spec=pltpu.PrefetchScalarGridSpec(
            num_scalar_prefetch=1, grid=(N_CHIPS,),
            in_specs=[pl.BlockSpec((rows, cols), lambda k, c: (2 * k + c[0], 0)) for _, rows, cols in sections]
            + slot,
            out_specs=tuple(slot)),
        compiler_params=_params(dimension_semantics=("parallel",)),
    )(core, *grads, *got)


def _chip_copies(sections, p_refs, land, send_sems, recv_sems):
    ns = len(sections)
    x, y, c = _position()
    return [pltpu.make_async_remote_copy(
        src_ref=p_refs[s].at[2 * cx + cy], dst_ref=land[s].at[j],
        send_sem=send_sems.at[j * ns + s], recv_sem=recv_sems.at[j * ns + s],
        device_id=(cx, cy, c), device_id_type=MESH_ID)
        for j, (cx, cy) in enumerate(_other_chips(x, y)) for s in range(ns)]


def _chip_exchange(sections, parts, name):
    ns = len(sections)

    def body(*refs):
        copies = _chip_copies(sections, refs[:ns], refs[ns:2 * ns], *refs[2 * ns:])
        for cp in copies:
            cp.start()
        for cp in copies:
            cp.wait_recv()
        for cp in copies:
            cp.wait_send()

    n = 3 * ns
    return pl.pallas_call(
        body, name=name,
        out_shape=tuple(jax.ShapeDtypeStruct((3, rows, cols), BF16) for _, rows, cols in sections),
        in_specs=[ANY] * ns, out_specs=(ANY,) * ns,
        scratch_shapes=[pltpu.SemaphoreType.DMA((n,)), pltpu.SemaphoreType.DMA((n,))],
    )(*parts)


def _chip_exchange_start(sections, parts, name):
    ns = len(sections)

    def body(*refs):
        p_refs, land = refs[:ns], refs[ns:2 * ns]
        send_sems, recv_sems = refs[2 * ns], refs[2 * ns + 1]
        token = refs[-1]
        for cp in _chip_copies(sections, p_refs, land, send_sems, recv_sems):
            cp.start()
        token[...] = jnp.zeros_like(token)

    zones = [lax.empty((3, rows, cols), BF16) for _, rows, cols in sections]
    out = pl.pallas_call(
        body, name=name,
        out_shape=(pltpu.SemaphoreType.DMA((3 * ns,)), pltpu.SemaphoreType.DMA((3 * ns,)),
                   *[pltpu.HBM(a.shape, a.dtype) for a in parts], *[pltpu.HBM(a.shape, a.dtype) for a in zones],
                   jax.ShapeDtypeStruct((8, LANES), F32)),
        in_specs=[HBM] * (2 * ns),
        out_specs=(SEM, SEM, *[HBM] * (2 * ns), pl.BlockSpec(memory_space=pltpu.VMEM)),
        input_output_aliases={i: 2 + i for i in range(2 * ns)},
        compiler_params=pltpu.CompilerParams(has_side_effects=EFFECT),
    )(*[_in_hbm(a) for a in parts], *[_in_hbm(a) for a in zones])
    return out[0], out[1], out[2:2 + ns], out[2 + ns:2 + 2 * ns], out[-1]


def _chip_exchange_wait(sections, send_sems, recv_sems, parts, zones, after, name):
    ns = len(sections)

    def body(*refs):
        p_refs, land = refs[:ns], refs[ns:2 * ns]
        for cp in _chip_copies(sections, p_refs, land, refs[2 * ns], refs[2 * ns + 1]):
            cp.wait_send()
            cp.wait_recv()

    out = pl.pallas_call(
        body, name=name,
        out_shape=tuple(pltpu.HBM(a.shape, a.dtype) for a in (*parts, *zones)),
        in_specs=[HBM] * (2 * ns) + [SEM, SEM, ANY],
        out_specs=(HBM,) * (2 * ns),
        input_output_aliases={i: i for i in range(2 * ns)},
        compiler_params=pltpu.CompilerParams(has_side_effects=EFFECT),
    )(*parts, *zones, send_sems, recv_sems, after)
    return out[:ns], out[ns:]


def _grad_finish(sections, parts, far, chip, name):
    ns = len(sections)

    def body(chip_ref, *refs):
        p_refs, b_refs, g_refs = refs[:ns], refs[ns:2 * ns], refs[2 * ns:]
        for s in range(ns):
            g = p_refs[s][0].astype(F32)
            for j in range(3):
                g = g + b_refs[s][j].astype(F32)
            g_refs[s][...] = g

    half = [(rows // 2, cols) for _, rows, cols in sections]
    return pl.pallas_call(
        body, name=name,
        out_shape=tuple(jax.ShapeDtypeStruct((rows, cols), F32) for _, rows, cols in sections),
        grid_spec=pltpu.PrefetchScalarGridSpec(
            num_scalar_prefetch=1, grid=(2,),
            in_specs=[pl.BlockSpec((1, r, c), lambda i, chip: (chip[0], i, 0)) for r, c in half]
            + [pl.BlockSpec((3, r, c), lambda i, chip: (0, i, 0)) for r, c in half],
            out_specs=tuple(pl.BlockSpec((r, c), lambda i, chip: (i, 0)) for r, c in half)),
        compiler_params=_params(dimension_semantics=("parallel",)),
    )(chip, *parts, *far)


def _sum_devices(parts, rows, name):
    cols = parts.shape[1]
    tr = rows // 2

    def body(*refs):
        s = refs[0][...].astype(F32)
        for d in range(1, N_DEV):
            s = s + refs[d][...].astype(F32)
        refs[N_DEV][...] = s

    return pl.pallas_call(
        body, name=name,
        out_shape=jax.ShapeDtypeStruct((rows, cols), F32),
        grid=(2,),
        in_specs=[pl.BlockSpec((tr, cols), lambda i, d=d: (2 * d + i, 0)) for d in range(N_DEV)],
        out_specs=pl.BlockSpec((tr, cols), lambda i: (i, 0)),
        compiler_params=_params(dimension_semantics=("parallel",)),
    )(*([parts] * N_DEV))


def _adamw_step(w_ref, g_ref, m_ref, v_ref, d_ref, nm_ref, nv_ref):
    c1 = 1.0 / (1.0 - ADAM_B1 ** ADAM_STEP)
    c2 = 1.0 / (1.0 - ADAM_B2 ** ADAM_STEP)
    gv = g_ref[...]
    nm = ADAM_B1 * m_ref[...] + (1.0 - ADAM_B1) * gv
    nv = ADAM_B2 * v_ref[...] + (1.0 - ADAM_B2) * (gv * gv)
    nm_ref[...] = nm
    nv_ref[...] = nv
    d_ref[...] = (-ADAM_LR) * ((nm * c1) / (jnp.sqrt(nv * c2) + ADAM_EPS) + ADAM_WD * w_ref[...])


def _adamw_small(params, name):
    n = len(params)

    def body(*refs):
        for k in range(n):
            _adamw_step(*refs[4 * k:4 * k + 4], *refs[4 * n + 3 * k:4 * n + 3 * k + 3])

    out = pl.pallas_call(
        body, name=name,
        out_shape=tuple(jax.ShapeDtypeStruct(p[0].shape, F32) for p in params for _ in range(3)),
    )(*[a for p in params for a in p])
    return [out[3 * k:3 * k + 3] for k in range(n)]


def _adamw(w, g, m, v, name):
    rows, cols = w.shape
    tr = rows
    while tr * cols * 4 > (1 << 20) and tr % 16 == 0:
        tr //= 2

    def body(*refs):
        _adamw_step(*refs)

    spec = pl.BlockSpec((tr, cols), lambda i: (i, 0))
    shape = jax.ShapeDtypeStruct((rows, cols), F32)
    return pl.pallas_call(
        body, name=name,
        out_shape=(shape, shape, shape),
        grid=(rows // tr,),
        in_specs=[spec] * 4, out_specs=(spec,) * 3,
        compiler_params=_params(dimension_semantics=("parallel",)),
    )(w, g, m, v)


def _adamw_from_parts(w, m, v, parts, far, chip, name):
    rows, cols = w.shape
    tr = rows
    while tr * cols * 4 > (1 << 20) and tr % 32 == 0:
        tr //= 2

    def body(chip_ref, w_ref, m_ref, v_ref, p_ref, b_ref, g_ref, d_ref, nm_ref, nv_ref):
        g = p_ref[0].astype(F32)
        for j in range(3):
            g = g + b_ref[j].astype(F32)
        g_ref[...] = g
        _adamw_step(w_ref, g_ref, m_ref, v_ref, d_ref, nm_ref, nv_ref)

    spec = pl.BlockSpec((tr, cols), lambda i, c: (i, 0))
    shape = jax.ShapeDtypeStruct((rows, cols), F32)
    return pl.pallas_call(
        body, name=name,
        out_shape=(shape,) * 4,
        grid_spec=pltpu.PrefetchScalarGridSpec(
            num_scalar_prefetch=1, grid=(rows // tr,),
            in_specs=[spec, spec, spec,
                      pl.BlockSpec((1, tr, cols), lambda i, c: (c[0], i, 0)),
                      pl.BlockSpec((3, tr, cols), lambda i, c: (0, i, 0))],
            out_specs=(spec,) * 4),
        compiler_params=_params(dimension_semantics=("parallel",)),
    )(chip, w, m, v, parts, far)


NAMES = ("ln1_g", "w_in", "b_in", "rpb", "w_att_o", "conv_w", "conv_b", "w_rg_a", "b_rg_a", "w_rg_i",
         "b_rg_i", "lru_lambda", "w_rec_o", "w_out", "ln2_g", "w_ff1", "w_ff2", "lnf_g")
TRANSPOSED = {"w_in": "w_in_t", "w_att_o": "w_att_o_t", "w_ff1": "w_ff1_t"}
ROW_SHARDED = ("w_rec_o", "w_out", "w_ff2")
REPLICATED = (("ln1_g", (1, D)), ("b_in", (1, D_IN)), ("rpb", (N_HEADS * N_RPB_R, N_RPB_C)),
              ("conv_b", (1, D_REC)), ("w_rg_a", (2 * N_REC_BLOCKS * REC_BLOCK, REC_BLOCK)),
              ("w_rg_i", (2 * N_REC_BLOCKS * REC_BLOCK, REC_BLOCK)), ("ln2_g", (1, D)), ("lnf_g", (1, D)))
GATE_BLOCKS = ("w_rg_a", "w_rg_i")
SMALL_ROWS = 112


def _chan_bits(vectors):
    chan = jnp.concatenate(vectors, axis=0)
    bits = lax.bitcast_convert_type(chan, BF16).reshape(-1)
    return jnp.pad(bits, (0, CHAN_BLOCK_ROWS * D - bits.shape[0])).reshape(CHAN_BLOCK_ROWS, D)


def _chan_from_bits(gathered):
    bits = gathered.reshape(N_DEV, CHAN_BLOCK_ROWS * D)[:, :2 * N_CHAN_ROWS * LANES]
    chan = lax.bitcast_convert_type(bits.reshape(N_DEV, N_CHAN_ROWS, LANES, 2), F32)
    return chan.transpose(1, 0, 2).reshape(N_CHAN_ROWS, D)


def kernel(x, ln1_g, w_in, b_in, rpb, w_att_o, conv_w, conv_b, w_rg_a, b_rg_a, w_rg_i, b_rg_i, lru_lambda, w_rec_o, w_out, ln2_g, w_ff1, w_ff2, lnf_g, loss_target, m_ln1_g, m_w_in, m_b_in, m_rpb, m_w_att_o, m_conv_w, m_conv_b, m_w_rg_a, m_b_rg_a, m_w_rg_i, m_b_rg_i, m_lru_lambda, m_w_rec_o, m_w_out, m_ln2_g, m_w_ff1, m_w_ff2, m_lnf_g, v_ln1_g, v_w_in, v_b_in, v_rpb, v_w_att_o, v_conv_w, v_conv_b, v_w_rg_a, v_b_rg_a, v_w_rg_i, v_b_rg_i, v_lru_lambda, v_w_rec_o, v_w_out, v_ln2_g, v_w_ff1, v_w_ff2, v_lnf_g):
    w = dict(zip(NAMES, (ln1_g, w_in, b_in, rpb, w_att_o, conv_w, conv_b, w_rg_a, b_rg_a, w_rg_i,
                         b_rg_i, lru_lambda, w_rec_o, w_out, ln2_g, w_ff1, w_ff2, lnf_g)))
    m = dict(zip(NAMES, (m_ln1_g, m_w_in, m_b_in, m_rpb, m_w_att_o, m_conv_w, m_conv_b, m_w_rg_a,
                         m_b_rg_a, m_w_rg_i, m_b_rg_i, m_lru_lambda, m_w_rec_o, m_w_out, m_ln2_g,
                         m_w_ff1, m_w_ff2, m_lnf_g)))
    v = dict(zip(NAMES, (v_ln1_g, v_w_in, v_b_in, v_rpb, v_w_att_o, v_conv_w, v_conv_b, v_w_rg_a,
                         v_b_rg_a, v_w_rg_i, v_b_rg_i, v_lru_lambda, v_w_rec_o, v_w_out, v_ln2_g,
                         v_w_ff1, v_w_ff2, v_lnf_g)))
    xi, yi, ci = _position()

    shard = {t: w[n][0].T.astype(BF16) for n, t in TRANSPOSED.items()}
    shard.update({n: w[n][0].astype(BF16) for n in ROW_SHARDED})
    shard["chan"] = _chan_bits([w[n][0] for n, _ in CHAN])
    first, later = ("w_in_t", "chan"), ("w_rec_o", "w_out", "w_att_o_t", "w_ff1_t", "w_ff2")
    *gathered, done = _all_gather([shard[n] for n in first], "weight_all_gather")
    p = dict(zip(first, gathered))
    send_sems, recv_sems, sent, zones, token = _gather_start([shard[n] for n in later], done,
                                                             "weight_gather_start")

    travelling = {"shards": sent, "zones": zones}
    stages = (("w_rec_o", "w_out", "w_att_o_t"), ("w_ff1_t", "w_ff2"))

    def late_weights(after, stage):
        which = [later.index(n) for n in stages[stage]]
        travelling["shards"], travelling["zones"] = _gather_wait(
            send_sems, recv_sems, travelling["shards"], travelling["zones"], which, after,
            "weight_gather_wait_%d" % stage)
        return dict(zip(stages[stage], _gather_pass_on(
            [shard[n].shape[0] for n in stages[stage]], [travelling["zones"][i] for i in which],
            PASS_ON_IDS[stage], "weight_gather_pass_on_%d" % stage)))

    chan = _chan_from_bits(p.pop("chan"))
    r0 = 0
    for n, rows in CHAN:
        p[n] = chan[r0:r0 + rows]
        r0 += rows
    p.update(ln1_g=w["ln1_g"], b_in=w["b_in"] + token[0, 0], rpb=w["rpb"][0], conv_b=w["conv_b"],
             w_rg_a=w["w_rg_a"][0], w_rg_i=w["w_rg_i"][0], ln2_g=w["ln2_g"],
             lnf_g=w["lnf_g"].reshape(1, D))

    core = jnp.reshape(ci, (1,)).astype(jnp.int32)
    chip = jnp.reshape(2 * xi + yi, (1,)).astype(jnp.int32)
    first_sections = tuple(s for s in SECTIONS if s[0] in ("w_ff1_t", "w_ff2"))
    late_sections = SECTIONS[:1]
    early_sections = tuple(s for s in SECTIONS[1:] if s not in first_sections)
    in_flight = {}

    def pair_sum_and_send(group, sections, after):
        send_sems, recv_sems, sect, zones, _ = in_flight["pair_" + group]
        sect, got = _pair_exchange_wait(sections, send_sems, recv_sems, sect, zones, after,
                                        "grad_pair_exchange_wait_" + group)
        parts = _pair_add(sections, sect, got, core, "grad_pair_add_" + group)
        in_flight[group] = _chip_exchange_start(sections, parts, "grad_chip_exchange_start_" + group)
        return in_flight[group][-1]

    def pair_exchange_at_once(group, sections, grads, barrier_id):
        in_flight["pair_" + group] = _pair_exchange_start(
            sections, [grads[n] for n, _, _ in sections], barrier_id, "grad_pair_exchange_start_" + group)
        return pair_sum_and_send(group, sections, in_flight["pair_" + group][-1])

    def reduce_first(grads, after):
        if grads is None:
            return pair_sum_and_send("first", first_sections, after)
        in_flight["pair_first"] = _pair_exchange_start(
            first_sections, [grads[n] for n, _, _ in first_sections], PAIR_FIRST_ID,
            "grad_pair_exchange_start_first")
        return in_flight["pair_first"][-1]

    def reduce_early(grads):
        chan_g = jnp.concatenate([grads[n] for n, _ in CHAN], axis=0)
        chan_g = chan_g.reshape(N_CHAN_ROWS, N_DEV, LANES).transpose(1, 0, 2).astype(BF16)
        chan_g = jnp.pad(chan_g.reshape(N_DEV, -1), ((0, 0), (0, CHAN_BLOCK_ROWS * D - N_CHAN_ROWS * LANES)))
        grads["chan"] = chan_g.reshape(N_DEV * CHAN_BLOCK_ROWS, D)
        grads["gates"] = jnp.concatenate([grads[n].reshape(-1, D) for n in GATE_BLOCKS], axis=0).astype(BF16)
        return pair_exchange_at_once("early", early_sections, grads, PAIR_EARLY_ID)[0, 0]

    loss_part, grad_x, grads = _local_step(x[0], loss_target[0], p, late_weights, reduce_first, reduce_early)
    in_flight["pair_late"] = _pair_exchange_start(
        late_sections, [grads[n] for n, _, _ in late_sections], PAIR_LATE_ID, "grad_pair_exchange_start_late")

    arrived = {}

    def finish(group, sections, after, name):
        send_sems, recv_sems, parts, zones, _ = in_flight[group]
        parts, far = _chip_exchange_wait(sections, send_sems, recv_sems, parts, zones, after,
                                         "grad_chip_exchange_wait_" + name)
        keep = [i for i, s in enumerate(sections) if s[0] not in (*ROW_SHARDED, "w_in_t")]
        arrived.update({s[0]: (parts[i], far[i]) for i, s in enumerate(sections) if i not in keep})
        if not keep:
            return {}
        return dict(zip((sections[i][0] for i in keep),
                        _grad_finish([sections[i] for i in keep], [parts[i] for i in keep],
                                     [far[i] for i in keep], chip, "grad_finish_" + name)))

    summed = finish("first", first_sections, in_flight["pair_late"][-1], "first")
    summed.update(finish("early", early_sections, summed["w_ff1_t"], "early"))
    started_late = pair_sum_and_send("late", late_sections, summed["gates"])

    flat = jnp.concatenate([grads[n].reshape(-1) for n, _ in REPLICATED if n not in GATE_BLOCKS]
                           + [loss_part.reshape(-1) + started_late[0, 0]])
    n_small = flat.shape[0]
    flat = jnp.pad(flat, (0, SMALL_ROWS * LANES - n_small)).reshape(SMALL_ROWS, LANES)
    small_parts, gate_sum, _ = _all_gather([flat, summed["gates"]], "small_grad_all_gather")
    small = _sum_devices(small_parts, SMALL_ROWS, "small_grad_sum").reshape(-1)
    loss = small[n_small - 1]

    g, delta, new_m, new_v = {}, {}, {}, {}

    def update(n, g2, shape2):
        d2, m2, v2 = _adamw(w[n].reshape(shape2), g2, m[n].reshape(shape2), v[n].reshape(shape2),
                            "adamw_" + n)
        g[n], delta[n], new_m[n], new_v[n] = (a.reshape(w[n].shape) for a in (g2, d2, m2, v2))

    small_params = []
    o = 0
    for n, shape2 in REPLICATED:
        if n in GATE_BLOCKS:
            k, rows = GATE_BLOCKS.index(n), gate_sum.shape[0] // len(GATE_BLOCKS)
            update(n, gate_sum[k * rows:(k + 1) * rows].reshape(shape2), shape2)
        else:
            size = shape2[0] * shape2[1]
            small_params.append((n, small[o:o + size].reshape(shape2), shape2))
            o += size
    chan_back = summed["chan"].reshape(-1)[:N_CHAN_ROWS * LANES].reshape(N_CHAN_ROWS, LANES)
    r0 = 0
    for n, rows in CHAN:
        small_params.append((n, chan_back[r0:r0 + rows], (rows, LANES)))
        r0 += rows
    results = _adamw_small([(w[n].reshape(s2), g2, m[n].reshape(s2), v[n].reshape(s2))
                            for n, g2, s2 in small_params], "adamw_vectors")
    for (n, g2, _), (d2, m2, v2) in zip(small_params, results):
        g[n], delta[n], new_m[n], new_v[n] = (a.reshape(w[n].shape) for a in (g2, d2, m2, v2))

    for n in ROW_SHARDED:
        results = _adamw_from_parts(w[n][0], m[n][0], v[n][0], *arrived[n], chip, "adamw_" + n)
        g[n], delta[n], new_m[n], new_v[n] = (a[None] for a in results)
    for n, t in TRANSPOSED.items():
        if t in summed:
            update(n, summed[t].T, summed[t].shape[::-1])
    finish("late", late_sections, _after_all(list(delta.values()), "updates_done"), "late")
    results = _adamw_from_parts(w["w_in"][0].T, m["w_in"][0].T, v["w_in"][0].T, *arrived["w_in_t"], chip,
                                "adamw_w_in")
    g["w_in"], delta["w_in"], new_m["w_in"], new_v["w_in"] = (a.T[None] for a in results)

    return (loss, grad_x[None], *[g[n] for n in NAMES], *[delta[n] for n in NAMES],
            *[new_m[n] for n in NAMES], *[new_v[n] for n in NAMES])
```

```python
import math

import numpy as np
import jax
import jax.numpy as jnp
from jax import lax
from jax.experimental import pallas as pl
from jax.experimental.pallas import tpu as pltpu

F32 = jnp.float32
BF16 = jnp.bfloat16

T = 2048
D = 1024
D_ATT = 512
D_REC = 1024
D_FF = 4096
D_IN = 5632
N_HEADS = 8
DH = 64
GRID_W = 64
ROWS = T // GRID_W
WIN_H = 8
WIN_W = 16
KWIN = WIN_H * GRID_W
N_RPB_R = 2 * WIN_H - 1
N_RPB_C = 2 * WIN_W - 1
N_REC_BLOCKS = 16
REC_BLOCK = 64
CG = 128
N_CG = D_REC // CG
LRU_C = 8.0
EPS = 1e-6
N_DEV = 8
N_CHIPS = 4
LANES = 128

ADAM_LR = 0.001
ADAM_B1 = 0.9
ADAM_B2 = 0.999
ADAM_EPS = 1e-08
ADAM_WD = 0.01
ADAM_STEP = 10

MESH_AXES = ("x", "y", "c")
VMEM_LIMIT = 56 * 1024 * 1024

TILE = 512
DZ_ARRAYS = ((0, 3, 1), (3, 4, 2), (7, 4, 2))
N_DZ_TILES = D_IN // TILE


def _params(**kw):
    return pltpu.CompilerParams(vmem_limit_bytes=VMEM_LIMIT, **kw)


HG = 4
HQ = HG * GRID_W
HC = HG * DH


def _att_tables():
    rq = np.arange(GRID_W)
    kc = np.arange(KWIN) % GRID_W
    win_start = np.clip(rq - WIN_W // 2, 0, GRID_W - WIN_W)
    valid = (kc[None, :] >= win_start[:, None]) & (kc[None, :] < win_start[:, None] + WIN_W)
    same_head = (np.arange(HQ)[:, None] // GRID_W) == (np.arange(HC)[None, :] // DH)
    return valid.astype(np.float32), same_head.astype(np.float32)


def _pair_mask():
    half = np.arange(2 * DH) // DH
    return (half[:, None] == half[None, :]).astype(np.float32)


def _dup_table():
    return np.concatenate([np.eye(REC_BLOCK, dtype=np.float32)] * 2, axis=1)


def _sigmoid(x):
    return 0.5 * jnp.tanh(0.5 * x) + 0.5


def _softplus(x):
    return jnp.maximum(x, 0.0) + jnp.log(1.0 + jnp.exp(-jnp.abs(x)))


def _one_minus_square(log_a, a):
    x = 2.0 * log_a
    series = -x * (1.0 + x * (0.5 + x * (1.0 / 6.0)))
    return jnp.where(x > -0.02, series, 1.0 - a * a)


_GELU_C = math.sqrt(2.0 / math.pi)


def _gelu_and_grad(x):
    x2 = x * x
    inner = _GELU_C * (x + 0.044715 * x * x2)
    t = jnp.tanh(inner)
    g = 0.5 * x * (1.0 + t)
    dg = 0.5 * (1.0 + t) + 0.5 * x * (1.0 - t * t) * _GELU_C * (1.0 + 3.0 * 0.044715 * x2)
    return g, dg


def _dot(a, b):
    return jnp.dot(a, b, preferred_element_type=F32)


def _dot_nt(a, b):
    return lax.dot_general(a, b, (((1,), (1,)), ((), ())), preferred_element_type=F32)


def _dot_tn(a, b):
    return lax.dot_general(a, b, (((0,), (0,)), ((), ())), preferred_element_type=F32)


def _dot_exact(a, b):
    return jnp.dot(a, b, precision=lax.Precision.HIGHEST, preferred_element_type=F32)


def _shift_rows(x, s):
    n = x.shape[0]
    rows = lax.broadcasted_iota(jnp.int32, x.shape, 0)
    y = pltpu.roll(x, s % n, 0)
    if s > 0:
        return jnp.where(rows >= s, y, 0.0)
    return jnp.where(rows < n + s, y, 0.0)


def _rms_bwd(dh, xh, r, g):
    dxh = dh * g
    return r * (dxh - xh * jnp.mean(dxh * xh, axis=-1, keepdims=True))


def _matmul(a, b, mode, out_dtype, name, tm=512, tn=1024, tk=2048):
    if mode == "nn":
        (m, k), (k2, n) = a.shape, b.shape
    elif mode == "nt":
        (m, k), (n, k2) = a.shape, b.shape
    else:
        (k, m), (k2, n) = a.shape, b.shape
    assert k == k2
    tm, tn, tk = min(tm, m), min(tn, n), min(tk, k)
    assert m % tm == 0 and n % tn == 0 and k % tk == 0
    nk = k // tk
    dot = {"nn": _dot, "nt": _dot_nt, "tn": _dot_tn}[mode]

    def body(a_ref, b_ref, o_ref, acc):
        kk = pl.program_id(2)
        part = dot(a_ref[...].astype(BF16), b_ref[...].astype(BF16))
        if nk == 1:
            o_ref[...] = part.astype(out_dtype)
            return

        @pl.when(kk == 0)
        def _():
            acc[...] = part

        @pl.when(kk > 0)
        def _():
            acc[...] += part

        @pl.when(kk == nk - 1)
        def _():
            o_ref[...] = acc[...].astype(out_dtype)

    if mode == "tn":
        a_spec = pl.BlockSpec((tk, tm), lambda i, j, kk: (kk, i))
    else:
        a_spec = pl.BlockSpec((tm, tk), lambda i, j, kk: (i, kk))
    if mode == "nt":
        b_spec = pl.BlockSpec((tn, tk), lambda i, j, kk: (j, kk))
    else:
        b_spec = pl.BlockSpec((tk, tn), lambda i, j, kk: (kk, j))
    return pl.pallas_call(
        body, name=name,
        out_shape=jax.ShapeDtypeStruct((m, n), out_dtype),
        grid=(m // tm, n // tn, nk),
        in_specs=[a_spec, b_spec],
        out_specs=pl.BlockSpec((tm, tn), lambda i, j, kk: (i, j)),
        scratch_shapes=[pltpu.VMEM((tm, tn) if nk > 1 else (8, LANES), F32)],
        compiler_params=_params(dimension_semantics=("parallel", "parallel", "arbitrary")),
    )(a, b)


def _in_proj(x, g1, w_in_t, b_in):
    tm = 512

    def body(x_ref, g_ref, w_hbm, b_ref, qkv_ref, uy_ref, gg_ref, h_ref, w):
        @pl.when(pl.program_id(0) == 0)
        def _():
            pltpu.sync_copy(w_hbm, w)

        xv = x_ref[...]
        r = lax.rsqrt(jnp.mean(xv * xv, axis=-1, keepdims=True) + EPS)
        h = ((xv * r) * g_ref[...]).astype(BF16)
        h_ref[...] = h
        row0 = 0
        for ref in (qkv_ref, uy_ref, gg_ref):
            for c0 in range(0, ref.shape[1], TILE):
                z = _dot_nt(h, w[row0:row0 + TILE, :]) + b_ref[:, row0:row0 + TILE]
                ref[:, c0:c0 + TILE] = z.astype(ref.dtype)
                row0 += TILE

    tok = lambda width: pl.BlockSpec((tm, width), lambda i: (i, 0))
    return pl.pallas_call(
        body, name="in_proj",
        out_shape=(jax.ShapeDtypeStruct((T, 3 * D_ATT), BF16),
                   jax.ShapeDtypeStruct((T, 2 * D_REC), F32),
                   jax.ShapeDtypeStruct((T, 2 * D), F32),
                   jax.ShapeDtypeStruct((T, D), BF16)),
        grid=(T // tm,),
        in_specs=[tok(D), pl.BlockSpec((1, D), lambda i: (0, 0)), pl.BlockSpec(memory_space=pl.ANY),
                  pl.BlockSpec((1, D_IN), lambda i: (0, 0))],
        out_specs=(tok(3 * D_ATT), tok(2 * D_REC), tok(2 * D), tok(D)),
        scratch_shapes=[pltpu.VMEM((D_IN, D), BF16)],
        compiler_params=_params(dimension_semantics=("arbitrary",)),
    )(x, g1, w_in_t, b_in)


def _dz_specs(rows, tile_of, row_of):
    def spec(off, n, per_plane):
        def index(*ids):
            t = jnp.clip(tile_of(*ids) - off, 0, n - 1)
            return (t // per_plane, row_of(*ids), t % per_plane)
        return pl.BlockSpec((1, rows, TILE), index)
    return [spec(off, n, per) for off, n, per in DZ_ARRAYS]


def _dh_norm1_bwd(dz, w_in_t, x, g1, dx1):
    tm = 512

    def body(dqkv_ref, duy_ref, dgg_ref, w_hbm, x_ref, g_ref, dx1_ref, gx_ref, dg_ref, w):
        @pl.when(pl.program_id(0) == 0)
        def _():
            pltpu.sync_copy(w_hbm, w)
            dg_ref[...] = jnp.zeros_like(dg_ref)

        dh, row0 = None, 0
        for ref in (dqkv_ref, duy_ref, dgg_ref):
            for plane in range(ref.shape[0]):
                cols = ref.shape[2]
                part = _dot(ref[plane], w[row0:row0 + cols, :])
                dh = part if dh is None else dh + part
                row0 += cols
        xv = x_ref[...]
        r = lax.rsqrt(jnp.mean(xv * xv, axis=-1, keepdims=True) + EPS)
        xh = xv * r
        dg_ref[...] += jnp.sum(dh * xh, axis=0, keepdims=True)
        gx_ref[...] = dx1_ref[...] + _rms_bwd(dh, xh, r, g_ref[...])

    tok = pl.BlockSpec((tm, D), lambda i: (i, 0))
    vec = pl.BlockSpec((1, D), lambda i: (0, 0))
    planes = lambda a: pl.BlockSpec((a.shape[0], tm, a.shape[2]), lambda i: (0, i, 0))
    return pl.pallas_call(
        body, name="dh_norm1_bwd",
        out_shape=(jax.ShapeDtypeStruct((T, D), F32), jax.ShapeDtypeStruct((1, D), F32)),
        grid=(T // tm,),
        in_specs=[planes(a) for a in dz] + [pl.BlockSpec(memory_space=pl.ANY), tok, vec, tok],
        out_specs=(tok, vec),
        scratch_shapes=[pltpu.VMEM((D_IN, D), BF16)],
        compiler_params=_params(dimension_semantics=("arbitrary",)),
    )(*dz, w_in_t, x, g1, dx1)


def _grad_w_in(dz, h):
    def body(*refs):
        seg_refs = refs[:3]
        h_ref, gw_ref, gb_ref = refs[3:]
        j = pl.program_id(0)

        for s, (off, n, _) in enumerate(DZ_ARRAYS):
            @pl.when((j >= off) & (j < off + n))
            def _(s=s):
                a = seg_refs[s][0]
                gw_ref[...] = _dot_tn(a, h_ref[...]).astype(BF16)
                gb_ref[...] = jnp.sum(a.astype(F32), axis=0, keepdims=True)

    return pl.pallas_call(
        body, name="grad_w_in",
        out_shape=(jax.ShapeDtypeStruct((D_IN, D), BF16), jax.ShapeDtypeStruct((1, D_IN), F32)),
        grid=(N_DZ_TILES,),
        in_specs=_dz_specs(T, lambda j: j, lambda j: 0) + [pl.BlockSpec((T, D), lambda j: (0, 0))],
        out_specs=(pl.BlockSpec((TILE, D), lambda j: (j, 0)), pl.BlockSpec((1, TILE), lambda j: (0, j))),
        compiler_params=_params(dimension_semantics=("parallel",)),
    )(*dz, h)


def _rpb_rows(rpb):
    padded = jnp.pad(rpb, ((0, 0), (0, 0), (0, GRID_W - N_RPB_C)))
    rows = [padded[:, WIN_H - 1 - oi: 2 * WIN_H - 1 - oi].reshape(N_HEADS // HG, HG, KWIN)
            for oi in range(WIN_H)]
    return jnp.stack(rows, axis=0)


SKEW = KWIN - (WIN_W - 1)


MASKED = -1e30


def _bias_tiles(rows_ref, valid, bias_s):
    for oi in range(WIN_H):
        for hh in range(HG):
            row = jnp.broadcast_to(rows_ref[oi, 0, hh:hh + 1, :], (GRID_W, KWIN))
            tile = pltpu.roll(row, SKEW, 1, stride=1, stride_axis=0)
            bias_s[oi, hh * GRID_W:(hh + 1) * GRID_W, :] = jnp.where(valid, tile, MASKED)


def _bias_tile_grads(gb_s, flip, out_ref):
    for oi in range(WIN_H):
        for hh in range(HG):
            g = _dot_exact(flip, gb_s[oi, hh * GRID_W:(hh + 1) * GRID_W, :])
            back = pltpu.roll(g, KWIN - (GRID_W - WIN_W), 1, stride=1, stride_axis=0)
            out_ref[0, oi, hh:hh + 1, :] = jnp.sum(back, axis=0, keepdims=True)


def _rpb_fold(row_grads):
    g = row_grads.transpose(1, 0, 2, 3).reshape(WIN_H, N_HEADS, WIN_H, GRID_W)
    g = g.transpose(0, 2, 1, 3)

    def body(g_ref, o_ref):
        for dr in range(N_RPB_R):
            terms = [g_ref[oi, i] for oi in range(WIN_H) for i in range(WIN_H) if i - oi + WIN_H - 1 == dr]
            acc = terms[0]
            for term in terms[1:]:
                acc = acc + term
            o_ref[dr] = acc

    out = pl.pallas_call(
        body, name="rpb_fold",
        out_shape=jax.ShapeDtypeStruct((N_RPB_R, N_HEADS, GRID_W), F32),
    )(g)
    return out.transpose(1, 0, 2)[:, :, :N_RPB_C]


ATT_GROUPS = N_HEADS // HG
ATT_UNROLL = 8


def _stacked(rows64, same_head):
    return jnp.where(same_head, jnp.concatenate([rows64] * HG, axis=0), jnp.zeros((), BF16))


def _own_heads(stacked):
    head = lax.broadcasted_iota(jnp.int32, (GRID_W, HC), 1) // DH
    out = stacked[:GRID_W]
    for h in range(1, HG):
        out = jnp.where(head == h, stacked[h * GRID_W:(h + 1) * GRID_W], out)
    return out


def _att_scores(q_ref, k_ref, bias_ref, same_head, r):
    rs = jnp.clip(r - WIN_H // 2, 0, ROWS - WIN_H)
    oi = r - rs
    q0 = pl.multiple_of(r * GRID_W, GRID_W)
    k0 = pl.multiple_of(rs * GRID_W, GRID_W)
    q2 = _stacked(q_ref[pl.ds(q0, GRID_W), :] * (DH ** -0.5), same_head)
    kw = k_ref[pl.ds(k0, KWIN), :]
    s = _dot_nt(q2, kw) + bias_ref[oi]
    e = jnp.exp(s - jnp.max(s, axis=-1, keepdims=True))
    return e, 1.0 / jnp.sum(e, axis=-1, keepdims=True), q2, kw, q0, k0, oi


def _att_specs():
    col = lambda off: pl.BlockSpec((T, HC), lambda g: (0, g + off * ATT_GROUPS))
    tables = [pl.BlockSpec((WIN_H, 1, HG, KWIN), lambda g: (0, g, 0, 0)),
              pl.BlockSpec((GRID_W, KWIN), lambda g: (0, 0)),
              pl.BlockSpec((HQ, HC), lambda g: (0, 0))]
    return col, tables, pltpu.VMEM((WIN_H, HQ, KWIN), F32)


def _att_fwd(qkv, bias_rows):
    valid_np, same_head_np = _att_tables()

    def body(q_ref, k_ref, v_ref, rows_ref, valid_ref, head_ref, o_ref, bias_s):
        same_head = head_ref[...] > 0.5
        _bias_tiles(rows_ref, valid_ref[...] > 0.5, bias_s)

        def row(r, carry):
            e, rl, _, _, q0, k0, _ = _att_scores(q_ref, k_ref, bias_s, same_head, r)
            o2 = _dot((e * rl).astype(BF16), v_ref[pl.ds(k0, KWIN), :])
            o_ref[pl.ds(q0, GRID_W), :] = _own_heads(o2).astype(BF16)
            return carry

        lax.fori_loop(0, ROWS, row, 0, unroll=ATT_UNROLL)

    col, tables, tiles = _att_specs()
    return pl.pallas_call(
        body, name="att_fwd",
        out_shape=jax.ShapeDtypeStruct((T, D_ATT), BF16),
        grid=(ATT_GROUPS,),
        in_specs=[col(0), col(1), col(2)] + tables,
        out_specs=col(0),
        scratch_shapes=[tiles],
        compiler_params=_params(dimension_semantics=("parallel",)),
    )(qkv, qkv, qkv, bias_rows, jnp.asarray(valid_np), jnp.asarray(same_head_np))


def _att_bwd(qkv, bias_rows, datt, after):
    valid_np, same_head_np = _att_tables()

    def body(q_ref, k_ref, v_ref, do_ref, rows_ref, valid_ref, head_ref, flip_ref,
             dqkv_ref, grows_ref, dk_acc, dv_acc, bias_s, gb_s):
        same_head = head_ref[...] > 0.5
        dk_acc[...] = jnp.zeros_like(dk_acc)
        dv_acc[...] = jnp.zeros_like(dv_acc)
        gb_s[...] = jnp.zeros_like(gb_s)
        _bias_tiles(rows_ref, valid_ref[...] > 0.5, bias_s)

        def row(r, carry):
            e, rl, q2, kw, q0, k0, oi = _att_scores(q_ref, k_ref, bias_s, same_head, r)
            do2 = _stacked(do_ref[pl.ds(q0, GRID_W), :], same_head)
            vw = v_ref[pl.ds(k0, KWIN), :]
            p = e * rl
            dp = _dot_nt(do2, vw)
            ds = p * (dp - jnp.sum(dp * p, axis=-1, keepdims=True))
            p16 = p.astype(BF16)
            ds16 = ds.astype(BF16)
            dv_acc[pl.ds(k0, KWIN), :] += _dot_tn(p16, do2)
            dk_acc[pl.ds(k0, KWIN), :] += _dot_tn(ds16, q2)
            dq2 = _dot(ds16, kw) * (DH ** -0.5)
            dqkv_ref[0, pl.ds(q0, GRID_W), :] = _own_heads(dq2).astype(BF16)
            gb_s[oi] += ds
            return carry

        lax.fori_loop(0, ROWS, row, 0, unroll=ATT_UNROLL)
        dqkv_ref[1] = dk_acc[...].astype(BF16)
        dqkv_ref[2] = dv_acc[...].astype(BF16)
        _bias_tile_grads(gb_s, flip_ref[...], grows_ref)

    col, tables, tiles = _att_specs()
    return pl.pallas_call(
        body, name="att_bwd",
        out_shape=(jax.ShapeDtypeStruct((3, T, D_ATT), BF16),
                   jax.ShapeDtypeStruct((ATT_GROUPS, WIN_H, HG, KWIN), F32)),
        grid=(ATT_GROUPS,),
        in_specs=[col(0), col(1), col(2), col(0)] + tables + [pl.BlockSpec((GRID_W, GRID_W), lambda g: (0, 0))],
        out_specs=(pl.BlockSpec((3, T, HC), lambda g: (0, 0, g)),
                   pl.BlockSpec((1, WIN_H, HG, KWIN), lambda g: (g, 0, 0, 0))),
        scratch_shapes=[pltpu.VMEM((T, HC), F32), pltpu.VMEM((T, HC), F32), tiles, tiles],
        compiler_params=_params(dimension_semantics=("parallel",)),
    )(qkv, qkv, qkv, datt, bias_rows, jnp.asarray(valid_np) + after, jnp.asarray(same_head_np),
      jnp.asarray(np.eye(GRID_W, dtype=np.float32)[::-1].copy()))


def _conv_taps(up):
    return (_shift_rows(up, 2), _shift_rows(up, 1), up, _shift_rows(up, -1))


def _pair_block_diag(w_pair, dup, same_half):
    return jnp.where(same_half, _dot(w_pair.astype(BF16), dup), 0.0).astype(BF16)


def _gates(u, u16, wa, ba, wi, bi, lam):
    r = _sigmoid(_dot(u16, wa) + ba)
    ig = _sigmoid(_dot(u16, wi) + bi)
    sp = _softplus(-lam)
    log_a = (-LRU_C) * r * sp
    a = jnp.exp(log_a)
    mult2 = jnp.maximum(_one_minus_square(log_a, a), 0.0)
    return r, ig, sp, a, jnp.sqrt(mult2), mult2


SCAN_BLOCKS = 8


def _scans(jobs):
    c = jobs[0][0].shape[1]
    nblk = T // 8
    rows = lax.broadcasted_iota(jnp.int32, (8, c), 0)

    def block(a, b, reverse):
        for s in (1, 2, 4):
            if reverse:
                keep = rows < 8 - s
                a_s = jnp.where(keep, pltpu.roll(a, 8 - s, 0), 1.0)
                b_s = jnp.where(keep, pltpu.roll(b, 8 - s, 0), 0.0)
            else:
                keep = rows >= s
                a_s = jnp.where(keep, pltpu.roll(a, s, 0), 1.0)
                b_s = jnp.where(keep, pltpu.roll(b, s, 0), 0.0)
            b = a * b_s + b
            a = a * a_s
        return a, b

    def step(i, carry):
        out = []
        for (a_ref, b_ref, h_ref, reverse), h_prev in zip(jobs, carry):
            for u in range(SCAN_BLOCKS):
                blk = i * SCAN_BLOCKS + u
                if reverse:
                    blk = nblk - 1 - blk
                t0 = pl.multiple_of(blk * 8, 8)
                a, b = block(a_ref[pl.ds(t0, 8), :], b_ref[pl.ds(t0, 8), :], reverse)
                h = a * h_prev + b
                h_ref[pl.ds(t0, 8), :] = h
                h_prev = jnp.broadcast_to(h[0:1] if reverse else h[7:8], (8, c))
            out.append(h_prev)
        return tuple(out)

    lax.fori_loop(0, nblk // SCAN_BLOCKS, step, tuple(jnp.zeros((8, c), F32) for _ in jobs))


def _rec_specs():
    tok = lambda off: pl.BlockSpec((T, CG), lambda g: (0, g + off))
    per_ch = lambda rows: pl.BlockSpec((rows, CG), lambda g: (0, g))
    wspec = pl.BlockSpec((2, 1, CG, REC_BLOCK), lambda g: (0, g, 0, 0))
    const = lambda shape: pl.BlockSpec(shape, lambda g: (0, 0))
    return tok, per_ch, wspec, const


def _rec_fwd(uy, conv_w, conv_b, w_a, b_a, w_i, b_i, lam):
    tok, per_ch, wspec, const = _rec_specs()

    def body(up_ref, yb_ref, cw_ref, cb_ref, wa_ref, ba_ref, wi_ref, bi_ref, lam_ref, dup_ref, half_ref,
             hf_ref, hb_ref, yrec_ref, am_ref, bx_f, bx_b):
        dup = dup_ref[...]
        same_half = half_ref[...] > 0.5
        taps = _conv_taps(up_ref[...])
        u = cb_ref[...]
        for j in range(4):
            u = u + taps[j] * cw_ref[j:j + 1, :]
        u16 = u.astype(BF16)
        for d, bx_s in enumerate((bx_f, bx_b)):
            wa = _pair_block_diag(wa_ref[d, 0], dup, same_half)
            wi = _pair_block_diag(wi_ref[d, 0], dup, same_half)
            _, ig, _, a, mult, _ = _gates(u, u16, wa, ba_ref[d:d + 1, :], wi, bi_ref[d:d + 1, :],
                                       lam_ref[d:d + 1, :])
            am_ref[2 * d] = a
            am_ref[2 * d + 1] = mult
            bx_s[...] = mult * (ig * u)
        _scans([(am_ref.at[0], bx_f, hf_ref, False), (am_ref.at[2], bx_b, hb_ref, True)])
        gelu, _ = _gelu_and_grad(yb_ref[...])
        yrec_ref[...] = ((hf_ref[...] + hb_ref[...]) * gelu).astype(BF16)

    return pl.pallas_call(
        body, name="rec_fwd",
        out_shape=(jax.ShapeDtypeStruct((T, D_REC), F32), jax.ShapeDtypeStruct((T, D_REC), F32),
                   jax.ShapeDtypeStruct((T, D_REC), BF16), jax.ShapeDtypeStruct((4, T, D_REC), F32)),
        grid=(N_CG,),
        in_specs=[tok(0), tok(N_CG), per_ch(4), per_ch(1), wspec, per_ch(2), wspec, per_ch(2), per_ch(2),
                  const((REC_BLOCK, CG)), const((CG, CG))],
        out_specs=(tok(0), tok(0), tok(0), pl.BlockSpec((4, T, CG), lambda g: (0, 0, g))),
        scratch_shapes=[pltpu.VMEM((T, CG), F32)] * 2,
        compiler_params=_params(dimension_semantics=("parallel",)),
    )(uy, uy, conv_w, conv_b, w_a, b_a, w_i, b_i, lam,
      jnp.asarray(_dup_table(), BF16), jnp.asarray(_pair_mask()))


def _rec_bwd(uy, hf, hb, am, dyrec, conv_w, conv_b, w_a, b_a, w_i, b_i, lam):
    tok, per_ch, wspec, const = _rec_specs()

    def body(up_ref, yb_ref, hf_ref, hb_ref, am_ref, dy_ref, cw_ref, cb_ref, wa_ref, ba_ref, wi_ref, bi_ref,
             lam_ref, dup_ref, dupt_ref, half_ref,
             duy_ref, dcw_ref, dcb_ref, dwa_ref, dba_ref, dwi_ref, dbi_ref, dlam_ref,
             a_s0, a_s1, dh_s, g_s0, g_s1):
        dup = dup_ref[...]
        dup_t = dupt_ref[...]
        same_half = half_ref[...] > 0.5
        taps = _conv_taps(up_ref[...])
        u = cb_ref[...]
        for j in range(4):
            u = u + taps[j] * cw_ref[j:j + 1, :]
        u16 = u.astype(BF16)
        gelu, dgelu = _gelu_and_grad(yb_ref[...])
        dy = dy_ref[...]
        duy_ref[1] = (dy * (hf_ref[...] + hb_ref[...]) * dgelu).astype(BF16)
        dh_s[...] = dy * gelu
        a_s0[...] = _shift_rows(am_ref[0], -1)
        a_s1[...] = _shift_rows(am_ref[2], 1)
        _scans([(a_s0, dh_s, g_s0, True), (a_s1, dh_s, g_s1, False)])
        du = jnp.zeros((T, CG), F32)
        for d, g_s in enumerate((g_s0, g_s1)):
            reverse = d == 1
            wa = _pair_block_diag(wa_ref[d, 0], dup, same_half)
            wi = _pair_block_diag(wi_ref[d, 0], dup, same_half)
            lam_d = lam_ref[d:d + 1, :]
            r = _sigmoid(_dot(u16, wa) + ba_ref[d:d + 1, :])
            ig = _sigmoid(_dot(u16, wi) + bi_ref[d:d + 1, :])
            sp = _softplus(-lam_d)
            a, mult = am_ref[2 * d], am_ref[2 * d + 1]
            mult2 = mult * mult
            g = g_s[...]
            h_prev = _shift_rows(hb_ref[...], -1) if reverse else _shift_rows(hf_ref[...], 1)
            da = g * h_prev
            dmult = g * (ig * u)
            dig = g * mult * u
            du = du + g * mult * ig
            dmult_dlog = jnp.where(mult2 > 0.0, -(a * a) * lax.rsqrt(mult2), 0.0)
            dlog_a = da * a + dmult * dmult_dlog
            dr = dlog_a * ((-LRU_C) * sp)
            dsp = jnp.sum(dlog_a * ((-LRU_C) * r), axis=0, keepdims=True)
            dlam_ref[d:d + 1, :] = dsp * (-_sigmoid(-lam_d))
            dga = dr * r * (1.0 - r)
            dgi = dig * ig * (1.0 - ig)
            dga16 = dga.astype(BF16)
            dgi16 = dgi.astype(BF16)
            du = du + _dot_nt(dga16, wa) + _dot_nt(dgi16, wi)
            dwa_ref[d, 0] = _dot_exact(jnp.where(same_half, _dot_tn(u16, dga16), 0.0), dup_t)
            dwi_ref[d, 0] = _dot_exact(jnp.where(same_half, _dot_tn(u16, dgi16), 0.0), dup_t)
            dba_ref[d:d + 1, :] = jnp.sum(dga, axis=0, keepdims=True)
            dbi_ref[d:d + 1, :] = jnp.sum(dgi, axis=0, keepdims=True)
        dcb_ref[...] = jnp.sum(du, axis=0, keepdims=True)
        for j in range(4):
            dcw_ref[j:j + 1, :] = jnp.sum(du * taps[j], axis=0, keepdims=True)
        dup_in = (_shift_rows(du, -2) * cw_ref[0:1, :] + _shift_rows(du, -1) * cw_ref[1:2, :]
                  + du * cw_ref[2:3, :] + _shift_rows(du, 1) * cw_ref[3:4, :])
        duy_ref[0] = dup_in.astype(BF16)

    wshape = jax.ShapeDtypeStruct((2, N_CG, CG, REC_BLOCK), F32)
    vec = lambda rows: jax.ShapeDtypeStruct((rows, D_REC), F32)
    dup_np = _dup_table()
    return pl.pallas_call(
        body, name="rec_bwd",
        out_shape=(jax.ShapeDtypeStruct((2, T, D_REC), BF16),
                   vec(4), vec(1), wshape, vec(2), wshape, vec(2), vec(2)),
        grid=(N_CG,),
        in_specs=[tok(0), tok(N_CG), tok(0), tok(0), pl.BlockSpec((4, T, CG), lambda g: (0, 0, g)), tok(0),
                  per_ch(4), per_ch(1), wspec, per_ch(2), wspec, per_ch(2), per_ch(2),
                  const((REC_BLOCK, CG)), const((CG, REC_BLOCK)), const((CG, CG))],
        out_specs=(pl.BlockSpec((2, T, CG), lambda g: (0, 0, g)),
                   per_ch(4), per_ch(1), wspec, per_ch(2), wspec, per_ch(2), per_ch(2)),
        scratch_shapes=[pltpu.VMEM((T, CG), F32)] * 5,
        compiler_params=_params(dimension_semantics=("parallel",)),
    )(uy, uy, hf, hb, am, dyrec, conv_w, conv_b, w_a, b_a, w_i, b_i, lam,
      jnp.asarray(dup_np, BF16), jnp.asarray(dup_np.T.copy()), jnp.asarray(_pair_mask()))


TM_MIX = 256


def _mix_specs():
    tok = lambda width, blk=0: pl.BlockSpec((TM_MIX, width), lambda i: (i, blk))
    full = lambda shape: pl.BlockSpec(shape, lambda i: (0, 0))
    return tok, full


def _mix_fwd(x, att, yrec, gg, w_att_o_t, w_rec_o, w_out):
    tok, full = _mix_specs()

    def body(x_ref, att_ref, yr_ref, ga_ref, gr_ref, wao_ref, wro_ref, wo_ref, x1_ref, mixed_ref):
        y_att = _dot_nt(att_ref[...], wao_ref[...])
        y_rec = _dot(yr_ref[...], wro_ref[...])
        mixed = (_sigmoid(ga_ref[...]) * y_att + _sigmoid(gr_ref[...]) * y_rec).astype(BF16)
        mixed_ref[...] = mixed
        x1_ref[...] = x_ref[...] + _dot(mixed, wo_ref[...])

    return pl.pallas_call(
        body, name="mix_fwd",
        out_shape=(jax.ShapeDtypeStruct((T, D), F32), jax.ShapeDtypeStruct((T, D), BF16)),
        grid=(T // TM_MIX,),
        in_specs=[tok(D), tok(D_ATT), tok(D_REC), tok(D, 0), tok(D, 1),
                  full((D, D_ATT)), full((D_REC, D)), full((D, D))],
        out_specs=(tok(D), tok(D)),
        compiler_params=_params(dimension_semantics=("parallel",)),
    )(x, att, yrec, gg, gg, w_att_o_t, w_rec_o, w_out)


def _mix_bwd(dx1, att, yrec, gg, w_att_o_t, w_rec_o, w_out, after):
    tok, full = _mix_specs()

    def body(dx_ref, att_ref, yr_ref, ga_ref, gr_ref, wao_ref, wro_ref, wo_ref, after_ref,
             dgg_ref, dya_ref, dyr_ref, datt_ref, dyrp_ref):
        dmixed = _dot_nt(dx_ref[...].astype(BF16), wo_ref[...])
        y_att = _dot_nt(att_ref[...], wao_ref[...])
        y_rec = _dot(yr_ref[...], wro_ref[...])
        sa = _sigmoid(ga_ref[...])
        sr = _sigmoid(gr_ref[...])
        dgg_ref[0] = (dmixed * y_att * sa * (1.0 - sa)).astype(BF16)
        dgg_ref[1] = (dmixed * y_rec * sr * (1.0 - sr)).astype(BF16)
        dya = (dmixed * sa).astype(BF16)
        dyr = (dmixed * sr).astype(BF16)
        dya_ref[...] = dya
        dyr_ref[...] = dyr
        datt_ref[...] = _dot(dya, wao_ref[...]).astype(BF16)
        dyrp_ref[...] = _dot_nt(dyr, wro_ref[...])

    return pl.pallas_call(
        body, name="mix_bwd",
        out_shape=(jax.ShapeDtypeStruct((2, T, D), BF16),
                   jax.ShapeDtypeStruct((T, D), BF16), jax.ShapeDtypeStruct((T, D), BF16),
                   jax.ShapeDtypeStruct((T, D_ATT), BF16), jax.ShapeDtypeStruct((T, D_REC), F32)),
        grid=(T // TM_MIX,),
        in_specs=[tok(D), tok(D_ATT), tok(D_REC), tok(D, 0), tok(D, 1),
                  full((D, D_ATT)), full((D_REC, D)), full((D, D)), pl.BlockSpec(memory_space=pl.ANY)],
        out_specs=(pl.BlockSpec((2, TM_MIX, D), lambda i: (0, i, 0)),
                   tok(D), tok(D), tok(D_ATT), tok(D_REC)),
        compiler_params=_params(dimension_semantics=("parallel",)),
    )(dx1, att, yrec, gg, gg, w_att_o_t, w_rec_o, w_out, after)


TM_FFN = 256
FF_CHUNK = 1024


def _ffn_loss(x1, target, g2, gf, w_ff1_t, w_ff2):
    n_chunks = D_FF // FF_CHUNK

    def body(x1_ref, tg_ref, g2_ref, gf_ref, w1_hbm, w2_hbm,
             loss_ref, dx1_ref, h2_ref, act_ref, dpre_ref, dx2_ref, dg2_ref, dgf_ref,
             w1, w2, relu_s):
        i = pl.program_id(0)

        @pl.when(i == 0)
        def _():
            pltpu.sync_copy(w1_hbm, w1)
            pltpu.sync_copy(w2_hbm, w2)
            loss_ref[...] = jnp.zeros_like(loss_ref)
            dg2_ref[...] = jnp.zeros_like(dg2_ref)
            dgf_ref[...] = jnp.zeros_like(dgf_ref)

        x1v = x1_ref[...]
        r2 = lax.rsqrt(jnp.mean(x1v * x1v, axis=-1, keepdims=True) + EPS)
        xh2 = x1v * r2
        h2 = (xh2 * g2_ref[...]).astype(BF16)
        h2_ref[...] = h2
        x2 = x1v
        for c in range(n_chunks):
            ff = slice(c * FF_CHUNK, (c + 1) * FF_CHUNK)
            rl = jnp.maximum(_dot_nt(h2, w1[ff, :]), 0.0)
            relu_s[:, ff] = rl
            act = (rl * rl).astype(BF16)
            act_ref[:, ff] = act
            x2 = x2 + _dot(act, w2[ff, :])
        r3 = lax.rsqrt(jnp.mean(x2 * x2, axis=-1, keepdims=True) + EPS)
        xh3 = x2 * r3
        err = xh3 * gf_ref[...] - tg_ref[...]
        loss_ref[...] += 0.5 * jnp.sum(jnp.mean(err * err, axis=-1, keepdims=True))
        dy = err * (1.0 / D)
        dgf_ref[...] += jnp.sum(dy * xh3, axis=0, keepdims=True)
        dx2 = _rms_bwd(dy, xh3, r3, gf_ref[...])
        dx2_16 = dx2.astype(BF16)
        dx2_ref[...] = dx2_16
        dh2 = jnp.zeros((TM_FFN, D), F32)
        for c in range(n_chunks):
            ff = slice(c * FF_CHUNK, (c + 1) * FF_CHUNK)
            dpre = (_dot_nt(dx2_16, w2[ff, :]) * (2.0 * relu_s[:, ff])).astype(BF16)
            dpre_ref[:, ff] = dpre
            dh2 = dh2 + _dot(dpre, w1[ff, :])
        dg2_ref[...] += jnp.sum(dh2 * xh2, axis=0, keepdims=True)
        dx1_ref[...] = dx2 + _rms_bwd(dh2, xh2, r2, g2_ref[...])

    tok = lambda width: pl.BlockSpec((TM_FFN, width), lambda i: (i, 0))
    vec = pl.BlockSpec((1, D), lambda i: (0, 0))
    hbm = pl.BlockSpec(memory_space=pl.ANY)
    return pl.pallas_call(
        body, name="ffn_loss",
        out_shape=(jax.ShapeDtypeStruct((8, 128), F32), jax.ShapeDtypeStruct((T, D), F32),
                   jax.ShapeDtypeStruct((T, D), BF16), jax.ShapeDtypeStruct((T, D_FF), BF16),
                   jax.ShapeDtypeStruct((T, D_FF), BF16), jax.ShapeDtypeStruct((T, D), BF16),
                   jax.ShapeDtypeStruct((1, D), F32), jax.ShapeDtypeStruct((1, D), F32)),
        grid=(T // TM_FFN,),
        in_specs=[tok(D), tok(D), vec, vec, hbm, hbm],
        out_specs=(pl.BlockSpec((8, 128), lambda i: (0, 0)), tok(D), tok(D), tok(D_FF), tok(D_FF), tok(D),
                   vec, vec),
        scratch_shapes=[pltpu.VMEM((D_FF, D), BF16), pltpu.VMEM((D_FF, D), BF16),
                        pltpu.VMEM((TM_FFN, D_FF), F32)],
        compiler_params=_params(dimension_semantics=("arbitrary",)),
    )(x1, target, g2, gf, w_ff1_t, w_ff2)


def _local_step(x, target, p, late_weights, reduce_first, reduce_early):
    bias = _rpb_rows(p["rpb"])
    pairs = lambda w: w.reshape(2, N_CG, CG, REC_BLOCK)
    w_a, w_i = pairs(p["w_rg_a"]), pairs(p["w_rg_i"])
    rec_params = (p["conv_w"], p["conv_b"], w_a, p["b_rg_a"], w_i, p["b_rg_i"], p["lru_lambda"])

    qkv, uy, gg, h = _in_proj(x, p["ln1_g"], p["w_in_t"], p["b_in"])
    att = _att_fwd(qkv, bias)
    hf, hb, yrec, am = _rec_fwd(uy, *rec_params)
    p = {**p, **late_weights(yrec, 0)}
    x1, mixed = _mix_fwd(x, att, yrec, gg, p["w_att_o_t"], p["w_rec_o"], p["w_out"])
    p = {**p, **late_weights(x1, 1)}
    loss8, dx1, h2, act, dpre, dx2, g_ln2, g_lnf = _ffn_loss(
        x1, target, p["ln2_g"], p["lnf_g"], p["w_ff1_t"], p["w_ff2"])

    grads = {"ln2_g": g_ln2, "lnf_g": g_lnf,
             "w_ff1_t": _matmul(dpre, h2, "tn", BF16, "g_w_ff1"),
             "w_ff2": _matmul(act, dx2, "tn", BF16, "g_w_ff2")}
    dgg, dya, dyr, datt, dyrp = _mix_bwd(dx1, att, yrec, gg, p["w_att_o_t"], p["w_rec_o"], p["w_out"],
                                         reduce_first(grads, None))
    lam_after = rec_params[-1] + reduce_first(None, dgg)[0, 0]
    duy, g_cw, g_cb, g_wa, g_ba, g_wi, g_bi, g_lam = _rec_bwd(uy, hf, hb, am, dyrp, *rec_params[:-1], lam_after)
    blocks = lambda g: g.reshape(2, N_REC_BLOCKS, REC_BLOCK, REC_BLOCK)
    grads.update({
        "w_att_o_t": _matmul(dya, att, "tn", BF16, "g_w_att_o"),
        "conv_w": g_cw, "conv_b": g_cb, "w_rg_a": blocks(g_wa), "b_rg_a": g_ba,
        "w_rg_i": blocks(g_wi), "b_rg_i": g_bi, "lru_lambda": g_lam,
        "w_rec_o": _matmul(yrec, dyr, "tn", BF16, "g_w_rec_o"),
        "w_out": _matmul(mixed, dx1, "tn", BF16, "g_w_out"),
    })
    dqkv, gbias = _att_bwd(qkv, bias, datt, reduce_early(grads))
    dz = (dqkv, duy, dgg)
    grad_x, g_ln1 = _dh_norm1_bwd(dz, p["w_in_t"], x, p["ln1_g"], dx1)
    g_w_in_t, g_b_in = _grad_w_in(dz, h)
    grads.update(ln1_g=g_ln1, w_in_t=g_w_in_t, b_in=g_b_in, rpb=_rpb_fold(gbias))
    return loss8[0:1, 0:1], grad_x, grads


MESH_ID = pl.DeviceIdType.MESH
ANY = pl.BlockSpec(memory_space=pl.ANY)

CHAN_BLOCK_ROWS = 32
GATE_ROWS = 2 * 2 * N_REC_BLOCKS * REC_BLOCK * REC_BLOCK // (N_DEV * D)
SECTIONS = (("w_in_t", 704, D), ("w_rec_o", 128, D), ("w_out", 128, D), ("w_ff1_t", 512, D),
            ("w_ff2", 512, D), ("chan", CHAN_BLOCK_ROWS, D), ("w_att_o_t", 128, D_ATT),
            ("gates", GATE_ROWS, D))
N_SEC = len(SECTIONS)
N_CHAN_ROWS = 10
CHAN = (("conv_w", 4), ("b_rg_a", 2), ("b_rg_i", 2), ("lru_lambda", 2))


def _position():
    return lax.axis_index("x"), lax.axis_index("y"), lax.axis_index("c")


def _other_chips(x, y):
    return [(1 - x, y), (x, 1 - y), (1 - x, 1 - y)]


PASS_ON_IDS, PAIR_EARLY_ID, PAIR_LATE_ID, PAIR_FIRST_ID = (1, 4), 2, 3, 5


def _pair_handshake(x, y, c):
    barrier = pltpu.get_barrier_semaphore()
    pl.semaphore_signal(barrier, inc=1, device_id=(x, y, 1 - c), device_id_type=MESH_ID)
    pl.semaphore_wait(barrier, 1)


def _block_of(ref, dev, rows):
    return ref.at[pl.ds(pl.multiple_of(dev * rows, 16), rows)]


def _all_gather(shards, name):
    ns = len(shards)

    def body(*refs):
        x_refs, out_refs, done_ref = refs[:ns], refs[ns:2 * ns], refs[2 * ns]
        send_sems, recv_sems, local_sems = refs[2 * ns + 1:]
        done_ref[0, 0] = 0.0
        x, y, c = _position()
        me, sibling = (x, y, c), (x, y, 1 - c)
        x_nbr, y_nbr, diagonal = _other_chips(x, y)
        north = c == 1
        relay_from = (jnp.where(north, x_nbr[0], y_nbr[0]), jnp.where(north, x_nbr[1], y_nbr[1]))
        relay_to = (jnp.where(north, y_nbr[0], x_nbr[0]), jnp.where(north, y_nbr[1], x_nbr[1]))

        def rows(s, px, py, pc):
            return _block_of(out_refs[s], 4 * px + 2 * py + pc, shards[s].shape[0])

        def copy(k, s, block, to, from_shard=False):
            return pltpu.make_async_remote_copy(
                src_ref=x_refs[s] if from_shard else rows(s, *block), dst_ref=rows(s, *block),
                send_sem=send_sems.at[k * ns + s], recv_sem=recv_sems.at[k * ns + s],
                device_id=to, device_id_type=MESH_ID)

        sections = range(ns)
        mine = [pltpu.make_async_copy(x_refs[s], rows(s, *me), local_sems.at[s]) for s in sections]
        sent = [copy(k, s, me, to, True) for k, to in enumerate((sibling, (*x_nbr, c), (*y_nbr, c)))
                for s in sections]
        for cp in mine + sent:
            cp.start()
        for s in sections:
            copy(1, s, (*x_nbr, c), me).wait_recv()
            copy(2, s, (*y_nbr, c), me).wait_recv()
            sent += [copy(3, s, (*relay_from, c), (*relay_to, c)),
                     copy(4, s, (*x_nbr, c), sibling), copy(5, s, (*y_nbr, c), sibling)]
            for cp in sent[-3:]:
                cp.start()
        for s in sections:
            copy(3, s, (*diagonal, c), me).wait_recv()
            sent.append(copy(6, s, (*diagonal, c), sibling))
            sent[-1].start()
        for s in sections:
            copy(0, s, sibling, me).wait_recv()
            for k, chip in ((4, x_nbr), (5, y_nbr), (6, diagonal)):
                copy(k, s, (*chip, 1 - c), me).wait_recv()
        for cp in sent:
            cp.wait_send()
        for cp in mine:
            cp.wait()

    return pl.pallas_call(
        body, name=name,
        out_shape=tuple(jax.ShapeDtypeStruct((N_DEV * s.shape[0], s.shape[1]), s.dtype) for s in shards)
        + (jax.ShapeDtypeStruct((1, 1), F32),),
        in_specs=[ANY] * ns,
        out_specs=(ANY,) * ns + (pl.BlockSpec(memory_space=pltpu.SMEM),),
        scratch_shapes=[pltpu.SemaphoreType.DMA((7 * ns,)), pltpu.SemaphoreType.DMA((7 * ns,)),
                        pltpu.SemaphoreType.DMA((ns,))],
    )(*shards)


HBM = pl.BlockSpec(memory_space=pltpu.HBM)
SEM = pl.BlockSpec(memory_space=pltpu.SEMAPHORE)
EFFECT = pltpu.SideEffectType.DATAFLOW_SIDE_EFFECTING


def _in_hbm(a):
    return pltpu.with_memory_space_constraint(a, pltpu.HBM)


def _first_hop_copies(shards, x_refs, zones, send_sems, recv_sems):
    ns = len(shards)
    x, y, c = _position()
    targets = [(x, y, 1 - c)] + [(cx, cy, c) for cx, cy in _other_chips(x, y)]
    return [pltpu.make_async_remote_copy(
        src_ref=x_refs[s], dst_ref=_block_of(zones[s], 4 * x + 2 * y + c, shards[s].shape[0]),
        send_sem=send_sems.at[k * ns + s], recv_sem=recv_sems.at[k * ns + s],
        device_id=to, device_id_type=MESH_ID)
        for k, to in enumerate(targets) for s in range(ns)]


def _after_all(arrays, name):
    def body(*refs):
        refs[-1][...] = jnp.zeros_like(refs[-1])

    return pl.pallas_call(
        body, name=name,
        out_shape=jax.ShapeDtypeStruct((8, LANES), F32),
        in_specs=[pl.BlockSpec(memory_space=pl.ANY)] * len(arrays),
        out_specs=pl.BlockSpec(memory_space=pltpu.VMEM),
    )(*arrays)


def _own_blocks_placed(shards, after):
    ns = len(shards)
    x, y, c = _position()
    me = jnp.reshape(4 * x + 2 * y + c, (1,)).astype(jnp.int32)
    shards = [*shards[:-1], shards[-1] + after.astype(shards[-1].dtype)]

    def body(me_ref, *refs):
        for s in range(ns):
            refs[ns + s][...] = refs[s][...]

    return pl.pallas_call(
        body, name="own_blocks_placed",
        out_shape=tuple(jax.ShapeDtypeStruct((N_DEV * s.shape[0], s.shape[1]), s.dtype) for s in shards),
        grid_spec=pltpu.PrefetchScalarGridSpec(
            num_scalar_prefetch=1, grid=(1,),
            in_specs=[pl.BlockSpec(s.shape, lambda i, me: (0, 0)) for s in shards],
            out_specs=tuple(pl.BlockSpec(s.shape, lambda i, me: (me[0], 0)) for s in shards)),
        compiler_params=_params(dimension_semantics=("arbitrary",)),
    )(me, *shards)


def _gather_start(shards, after, name):
    ns = len(shards)
    zones = _own_blocks_placed(shards, after)

    def body(*refs):
        for cp in _first_hop_copies(shards, refs[:ns], refs[ns:2 * ns], refs[2 * ns], refs[2 * ns + 1]):
            cp.start()
        refs[-1][...] = jnp.zeros_like(refs[-1])

    out = pl.pallas_call(
        body, name=name,
        out_shape=(pltpu.SemaphoreType.DMA((4 * ns,)), pltpu.SemaphoreType.DMA((4 * ns,)),
                   *[pltpu.HBM(a.shape, a.dtype) for a in (*shards, *zones)],
                   jax.ShapeDtypeStruct((8, LANES), F32)),
        in_specs=[HBM] * (2 * ns),
        out_specs=(SEM, SEM, *[HBM] * (2 * ns), pl.BlockSpec(memory_space=pltpu.VMEM)),
        input_output_aliases={i: 2 + i for i in range(2 * ns)},
        compiler_params=pltpu.CompilerParams(has_side_effects=EFFECT),
    )(*[_in_hbm(a) for a in shards], *[_in_hbm(a) for a in zones])
    return out[0], out[1], out[2:2 + ns], out[2 + ns:2 + 2 * ns], out[-1]


def _gather_wait(send_sems, recv_sems, shards, zones, which, after, name):
    ns = len(shards)

    def body(*refs):
        copies = _first_hop_copies(shards, refs[:ns], refs[ns:2 * ns], refs[2 * ns], refs[2 * ns + 1])
        for i, cp in enumerate(copies):
            if i % ns in which:
                cp.wait_send()
                cp.wait_recv()

    out = pl.pallas_call(
        body, name=name,
        out_shape=tuple(pltpu.HBM(a.shape, a.dtype) for a in (*shards, *zones)),
        in_specs=[HBM] * (2 * ns) + [SEM, SEM, ANY],
        out_specs=(HBM,) * (2 * ns),
        input_output_aliases={i: i for i in range(2 * ns)},
        compiler_params=pltpu.CompilerParams(has_side_effects=EFFECT),
    )(*shards, *zones, send_sems, recv_sems, after)
    return out[:ns], out[ns:]


def _gather_pass_on(rows, zones, barrier_id, name):
    ns = len(zones)

    def body(*refs):
        in_refs, out_refs = refs[:ns], refs[ns:2 * ns]
        send_sems, recv_sems = refs[2 * ns:]
        x, y, c = _position()
        _pair_handshake(x, y, c)
        copies = [pltpu.make_async_remote_copy(
            src_ref=_block_of(in_refs[s], 4 * cx + 2 * cy + c, rows[s]),
            dst_ref=_block_of(out_refs[s], 4 * cx + 2 * cy + c, rows[s]),
            send_sem=send_sems.at[j * ns + s], recv_sem=recv_sems.at[j * ns + s],
            device_id=(x, y, 1 - c), device_id_type=MESH_ID)
            for j, (cx, cy) in enumerate(_other_chips(x, y)) for s in range(ns)]
        for cp in copies:
            cp.start()
        for cp in copies:
            cp.wait_recv()
        for cp in copies:
            cp.wait_send()

    return pl.pallas_call(
        body, name=name,
        out_shape=tuple(jax.ShapeDtypeStruct(z.shape, z.dtype) for z in zones),
        in_specs=[ANY] * ns, out_specs=(ANY,) * ns,
        input_output_aliases={i: i for i in range(ns)},
        scratch_shapes=[pltpu.SemaphoreType.DMA((3 * ns,)), pltpu.SemaphoreType.DMA((3 * ns,))],
        compiler_params=pltpu.CompilerParams(collective_id=barrier_id),
    )(*zones)


def _pair_copies(sections, g_refs, land, send_sems, recv_sems):
    ns = len(sections)
    x, y, c = _position()
    return [pltpu.make_async_remote_copy(
        src_ref=_block_of(g_refs[s], 2 * k + 1 - c, rows), dst_ref=land[s].at[k],
        send_sem=send_sems.at[k * ns + s], recv_sem=recv_sems.at[k * ns + s],
        device_id=(x, y, 1 - c), device_id_type=MESH_ID)
        for k in range(N_CHIPS) for s, (_, rows, _) in enumerate(sections)]


def _pair_exchange_start(sections, grads, barrier_id, name):
    ns = len(sections)

    def body(*refs):
        _pair_handshake(*_position())
        for cp in _pair_copies(sections, refs[:ns], refs[ns:2 * ns], refs[2 * ns], refs[2 * ns + 1]):
            cp.start()
        refs[-1][...] = jnp.zeros_like(refs[-1])

    zones = [lax.empty((N_CHIPS, rows, cols), BF16) for _, rows, cols in sections]
    n = N_CHIPS * ns
    out = pl.pallas_call(
        body, name=name,
        out_shape=(pltpu.SemaphoreType.DMA((n,)), pltpu.SemaphoreType.DMA((n,)),
                   *[pltpu.HBM(a.shape, a.dtype) for a in (*grads, *zones)],
                   jax.ShapeDtypeStruct((8, LANES), F32)),
        in_specs=[HBM] * (2 * ns),
        out_specs=(SEM, SEM, *[HBM] * (2 * ns), pl.BlockSpec(memory_space=pltpu.VMEM)),
        input_output_aliases={i: 2 + i for i in range(2 * ns)},
        compiler_params=pltpu.CompilerParams(has_side_effects=EFFECT, collective_id=barrier_id),
    )(*[_in_hbm(a) for a in grads], *[_in_hbm(a) for a in zones])
    return out[0], out[1], out[2:2 + ns], out[2 + ns:2 + 2 * ns], out[-1]


def _pair_exchange_wait(sections, send_sems, recv_sems, grads, zones, after, name):
    ns = len(sections)

    def body(*refs):
        for cp in _pair_copies(sections, refs[:ns], refs[ns:2 * ns], refs[2 * ns], refs[2 * ns + 1]):
            cp.wait_send()
            cp.wait_recv()

    out = pl.pallas_call(
        body, name=name,
        out_shape=tuple(pltpu.HBM(a.shape, a.dtype) for a in (*grads, *zones)),
        in_specs=[HBM] * (2 * ns) + [SEM, SEM, ANY],
        out_specs=(HBM,) * (2 * ns),
        input_output_aliases={i: i for i in range(2 * ns)},
        compiler_params=pltpu.CompilerParams(has_side_effects=EFFECT),
    )(*grads, *zones, send_sems, recv_sems, after)
    return out[:ns], out[ns:]


def _pair_add(sections, grads, got, core, name):
    ns = len(sections)

    def body(core_ref, *refs):
        g_refs, got_refs, p_refs = refs[:ns], refs[ns:2 * ns], refs[2 * ns:]
        for s in range(ns):
            p_refs[s][0] = (g_refs[s][...].astype(F32) + got_refs[s][0].astype(F32)).astype(BF16)

    slot = [pl.BlockSpec((1, rows, cols), lambda k, c: (k, 0, 0)) for _, rows, cols in sections]
    return pl.pallas_call(
        body, name=name,
        out_shape=tuple(jax.ShapeDtypeStruct((N_CHIPS, rows, cols), BF16) for _, rows, cols in sections),
        grid_spec=pltpu.PrefetchScalarGridSpec(
            num_scalar_prefetch=1, grid=(N_CHIPS,),
            in_specs=[pl.BlockSpec((rows, cols), lambda k, c: (2 * k + c[0], 0)) for _, rows, cols in sections]
            + slot,
            out_specs=tuple(slot)),
        compiler_params=_params(dimension_semantics=("parallel",)),
    )(core, *grads, *got)


def _chip_copies(sections, p_refs, land, send_sems, recv_sems):
    ns = len(sections)
    x, y, c = _position()
    return [pltpu.make_async_remote_copy(
        src_ref=p_refs[s].at[2 * cx + cy], dst_ref=land[s].at[j],
        send_sem=send_sems.at[j * ns + s], recv_sem=recv_sems.at[j * ns + s],
        device_id=(cx, cy, c), device_id_type=MESH_ID)
        for j, (cx, cy) in enumerate(_other_chips(x, y)) for s in range(ns)]


def _chip_exchange(sections, parts, name):
    ns = len(sections)

    def body(*refs):
        copies = _chip_copies(sections, refs[:ns], refs[ns:2 * ns], *refs[2 * ns:])
        for cp in copies:
            cp.start()
        for cp in copies:
            cp.wait_recv()
        for cp in copies:
            cp.wait_send()

    n = 3 * ns
    return pl.pallas_call(
        body, name=name,
        out_shape=tuple(jax.ShapeDtypeStruct((3, rows, cols), BF16) for _, rows, cols in sections),
        in_specs=[ANY] * ns, out_specs=(ANY,) * ns,
        scratch_shapes=[pltpu.SemaphoreType.DMA((n,)), pltpu.SemaphoreType.DMA((n,))],
    )(*parts)


def _chip_exchange_start(sections, parts, name):
    ns = len(sections)

    def body(*refs):
        p_refs, land = refs[:ns], refs[ns:2 * ns]
        send_sems, recv_sems = refs[2 * ns], refs[2 * ns + 1]
        token = refs[-1]
        for cp in _chip_copies(sections, p_refs, land, send_sems, recv_sems):
            cp.start()
        token[...] = jnp.zeros_like(token)

    zones = [lax.empty((3, rows, cols), BF16) for _, rows, cols in sections]
    out = pl.pallas_call(
        body, name=name,
        out_shape=(pltpu.SemaphoreType.DMA((3 * ns,)), pltpu.SemaphoreType.DMA((3 * ns,)),
                   *[pltpu.HBM(a.shape, a.dtype) for a in parts], *[pltpu.HBM(a.shape, a.dtype) for a in zones],
                   jax.ShapeDtypeStruct((8, LANES), F32)),
        in_specs=[HBM] * (2 * ns),
        out_specs=(SEM, SEM, *[HBM] * (2 * ns), pl.BlockSpec(memory_space=pltpu.VMEM)),
        input_output_aliases={i: 2 + i for i in range(2 * ns)},
        compiler_params=pltpu.CompilerParams(has_side_effects=EFFECT),
    )(*[_in_hbm(a) for a in parts], *[_in_hbm(a) for a in zones])
    return out[0], out[1], out[2:2 + ns], out[2 + ns:2 + 2 * ns], out[-1]


def _chip_exchange_wait(sections, send_sems, recv_sems, parts, zones, after, name):
    ns = len(sections)

    def body(*refs):
        p_refs, land = refs[:ns], refs[ns:2 * ns]
        for cp in _chip_copies(sections, p_refs, land, refs[2 * ns], refs[2 * ns + 1]):
            cp.wait_send()
            cp.wait_recv()

    out = pl.pallas_call(
        body, name=name,
        out_shape=tuple(pltpu.HBM(a.shape, a.dtype) for a in (*parts, *zones)),
        in_specs=[HBM] * (2 * ns) + [SEM, SEM, ANY],
        out_specs=(HBM,) * (2 * ns),
        input_output_aliases={i: i for i in range(2 * ns)},
        compiler_params=pltpu.CompilerParams(has_side_effects=EFFECT),
    )(*parts, *zones, send_sems, recv_sems, after)
    return out[:ns], out[ns:]


def _grad_finish(sections, parts, far, chip, name):
    ns = len(sections)

    def body(chip_ref, *refs):
        p_refs, b_refs, g_refs = refs[:ns], refs[ns:2 * ns], refs[2 * ns:]
        for s in range(ns):
            g = p_refs[s][0].astype(F32)
            for j in range(3):
                g = g + b_refs[s][j].astype(F32)
            g_refs[s][...] = g

    half = [(rows // 2, cols) for _, rows, cols in sections]
    return pl.pallas_call(
        body, name=name,
        out_shape=tuple(jax.ShapeDtypeStruct((rows, cols), F32) for _, rows, cols in sections),
        grid_spec=pltpu.PrefetchScalarGridSpec(
            num_scalar_prefetch=1, grid=(2,),
            in_specs=[pl.BlockSpec((1, r, c), lambda i, chip: (chip[0], i, 0)) for r, c in half]
            + [pl.BlockSpec((3, r, c), lambda i, chip: (0, i, 0)) for r, c in half],
            out_specs=tuple(pl.BlockSpec((r, c), lambda i, chip: (i, 0)) for r, c in half)),
        compiler_params=_params(dimension_semantics=("parallel",)),
    )(chip, *parts, *far)


def _sum_devices(parts, rows, name):
    cols = parts.shape[1]
    tr = rows // 2

    def body(*refs):
        s = refs[0][...].astype(F32)
        for d in range(1, N_DEV):
            s = s + refs[d][...].astype(F32)
        refs[N_DEV][...] = s

    return pl.pallas_call(
        body, name=name,
        out_shape=jax.ShapeDtypeStruct((rows, cols), F32),
        grid=(2,),
        in_specs=[pl.BlockSpec((tr, cols), lambda i, d=d: (2 * d + i, 0)) for d in range(N_DEV)],
        out_specs=pl.BlockSpec((tr, cols), lambda i: (i, 0)),
        compiler_params=_params(dimension_semantics=("parallel",)),
    )(*([parts] * N_DEV))


def _adamw_step(w_ref, g_ref, m_ref, v_ref, d_ref, nm_ref, nv_ref):
    c1 = 1.0 / (1.0 - ADAM_B1 ** ADAM_STEP)
    c2 = 1.0 / (1.0 - ADAM_B2 ** ADAM_STEP)
    gv = g_ref[...]
    nm = ADAM_B1 * m_ref[...] + (1.0 - ADAM_B1) * gv
    nv = ADAM_B2 * v_ref[...] + (1.0 - ADAM_B2) * (gv * gv)
    nm_ref[...] = nm
    nv_ref[...] = nv
    d_ref[...] = (-ADAM_LR) * ((nm * c1) / (jnp.sqrt(nv * c2) + ADAM_EPS) + ADAM_WD * w_ref[...])


def _adamw_small(params, name):
    n = len(params)

    def body(*refs):
        for k in range(n):
            _adamw_step(*refs[4 * k:4 * k + 4], *refs[4 * n + 3 * k:4 * n + 3 * k + 3])

    out = pl.pallas_call(
        body, name=name,
        out_shape=tuple(jax.ShapeDtypeStruct(p[0].shape, F32) for p in params for _ in range(3)),
    )(*[a for p in params for a in p])
    return [out[3 * k:3 * k + 3] for k in range(n)]


ADAMW_TILE_BYTES = 256 << 10


def _adamw_rows(rows, cols):
    fits = [tr for tr in range(16, rows + 1, 16) if rows % tr == 0 and tr * cols * 4 <= ADAMW_TILE_BYTES]
    return max(fits, default=rows)


def _adamw(w, g, m, v, name):
    rows, cols = w.shape
    tr = _adamw_rows(rows, cols)

    def body(*refs):
        _adamw_step(*refs)

    spec = pl.BlockSpec((tr, cols), lambda i: (i, 0))
    shape = jax.ShapeDtypeStruct((rows, cols), F32)
    return pl.pallas_call(
        body, name=name,
        out_shape=(shape, shape, shape),
        grid=(rows // tr,),
        in_specs=[spec] * 4, out_specs=(spec,) * 3,
        compiler_params=_params(dimension_semantics=("parallel",)),
    )(w, g, m, v)


def _adamw_from_parts(w, m, v, parts, far, chip, name):
    rows, cols = w.shape
    tr = _adamw_rows(rows, cols)

    def body(chip_ref, w_ref, m_ref, v_ref, p_ref, b_ref, g_ref, d_ref, nm_ref, nv_ref):
        g = p_ref[0].astype(F32)
        for j in range(3):
            g = g + b_ref[j].astype(F32)
        g_ref[...] = g
        _adamw_step(w_ref, g_ref, m_ref, v_ref, d_ref, nm_ref, nv_ref)

    spec = pl.BlockSpec((tr, cols), lambda i, c: (i, 0))
    shape = jax.ShapeDtypeStruct((rows, cols), F32)
    return pl.pallas_call(
        body, name=name,
        out_shape=(shape,) * 4,
        grid_spec=pltpu.PrefetchScalarGridSpec(
            num_scalar_prefetch=1, grid=(rows // tr,),
            in_specs=[spec, spec, spec,
                      pl.BlockSpec((1, tr, cols), lambda i, c: (c[0], i, 0)),
                      pl.BlockSpec((3, tr, cols), lambda i, c: (0, i, 0))],
            out_specs=(spec,) * 4),
        compiler_params=_params(dimension_semantics=("parallel",)),
    )(chip, w, m, v, parts, far)


NAMES = ("ln1_g", "w_in", "b_in", "rpb", "w_att_o", "conv_w", "conv_b", "w_rg_a", "b_rg_a", "w_rg_i",
         "b_rg_i", "lru_lambda", "w_rec_o", "w_out", "ln2_g", "w_ff1", "w_ff2", "lnf_g")
TRANSPOSED = {"w_in": "w_in_t", "w_att_o": "w_att_o_t", "w_ff1": "w_ff1_t"}
ROW_SHARDED = ("w_rec_o", "w_out", "w_ff2")
REPLICATED = (("ln1_g", (1, D)), ("b_in", (1, D_IN)), ("rpb", (N_HEADS * N_RPB_R, N_RPB_C)),
              ("conv_b", (1, D_REC)), ("w_rg_a", (2 * N_REC_BLOCKS * REC_BLOCK, REC_BLOCK)),
              ("w_rg_i", (2 * N_REC_BLOCKS * REC_BLOCK, REC_BLOCK)), ("ln2_g", (1, D)), ("lnf_g", (1, D)))
GATE_BLOCKS = ("w_rg_a", "w_rg_i")
SMALL_ROWS = 112


def _chan_bits(vectors):
    chan = jnp.concatenate(vectors, axis=0)
    bits = lax.bitcast_convert_type(chan, BF16).reshape(-1)
    return jnp.pad(bits, (0, CHAN_BLOCK_ROWS * D - bits.shape[0])).reshape(CHAN_BLOCK_ROWS, D)


def _chan_from_bits(gathered):
    bits = gathered.reshape(N_DEV, CHAN_BLOCK_ROWS * D)[:, :2 * N_CHAN_ROWS * LANES]
    chan = lax.bitcast_convert_type(bits.reshape(N_DEV, N_CHAN_ROWS, LANES, 2), F32)
    return chan.transpose(1, 0, 2).reshape(N_CHAN_ROWS, D)


def kernel(x, ln1_g, w_in, b_in, rpb, w_att_o, conv_w, conv_b, w_rg_a, b_rg_a, w_rg_i, b_rg_i, lru_lambda, w_rec_o, w_out, ln2_g, w_ff1, w_ff2, lnf_g, loss_target, m_ln1_g, m_w_in, m_b_in, m_rpb, m_w_att_o, m_conv_w, m_conv_b, m_w_rg_a, m_b_rg_a, m_w_rg_i, m_b_rg_i, m_lru_lambda, m_w_rec_o, m_w_out, m_ln2_g, m_w_ff1, m_w_ff2, m_lnf_g, v_ln1_g, v_w_in, v_b_in, v_rpb, v_w_att_o, v_conv_w, v_conv_b, v_w_rg_a, v_b_rg_a, v_w_rg_i, v_b_rg_i, v_lru_lambda, v_w_rec_o, v_w_out, v_ln2_g, v_w_ff1, v_w_ff2, v_lnf_g):
    w = dict(zip(NAMES, (ln1_g, w_in, b_in, rpb, w_att_o, conv_w, conv_b, w_rg_a, b_rg_a, w_rg_i,
                         b_rg_i, lru_lambda, w_rec_o, w_out, ln2_g, w_ff1, w_ff2, lnf_g)))
    m = dict(zip(NAMES, (m_ln1_g, m_w_in, m_b_in, m_rpb, m_w_att_o, m_conv_w, m_conv_b, m_w_rg_a,
                         m_b_rg_a, m_w_rg_i, m_b_rg_i, m_lru_lambda, m_w_rec_o, m_w_out, m_ln2_g,
                         m_w_ff1, m_w_ff2, m_lnf_g)))
    v = dict(zip(NAMES, (v_ln1_g, v_w_in, v_b_in, v_rpb, v_w_att_o, v_conv_w, v_conv_b, v_w_rg_a,
                         v_b_rg_a, v_w_rg_i, v_b_rg_i, v_lru_lambda, v_w_rec_o, v_w_out, v_ln2_g,
                         v_w_ff1, v_w_ff2, v_lnf_g)))
    xi, yi, ci = _position()

    shard = {t: w[n][0].T.astype(BF16) for n, t in TRANSPOSED.items()}
    shard.update({n: w[n][0].astype(BF16) for n in ROW_SHARDED})
    shard["chan"] = _chan_bits([w[n][0] for n, _ in CHAN])
    first, later = ("w_in_t", "chan"), ("w_rec_o", "w_out", "w_att_o_t", "w_ff1_t", "w_ff2")
    *gathered, done = _all_gather([shard[n] for n in first], "weight_all_gather")
    p = dict(zip(first, gathered))
    send_sems, recv_sems, sent, zones, token = _gather_start([shard[n] for n in later], done,
                                                             "weight_gather_start")

    travelling = {"shards": sent, "zones": zones}
    stages = (("w_rec_o", "w_out", "w_att_o_t"), ("w_ff1_t", "w_ff2"))

    def late_weights(after, stage):
        which = [later.index(n) for n in stages[stage]]
        travelling["shards"], travelling["zones"] = _gather_wait(
            send_sems, recv_sems, travelling["shards"], travelling["zones"], which, after,
            "weight_gather_wait_%d" % stage)
        return dict(zip(stages[stage], _gather_pass_on(
            [shard[n].shape[0] for n in stages[stage]], [travelling["zones"][i] for i in which],
            PASS_ON_IDS[stage], "weight_gather_pass_on_%d" % stage)))

    chan = _chan_from_bits(p.pop("chan"))
    r0 = 0
    for n, rows in CHAN:
        p[n] = chan[r0:r0 + rows]
        r0 += rows
    p.update(ln1_g=w["ln1_g"], b_in=w["b_in"] + token[0, 0], rpb=w["rpb"][0], conv_b=w["conv_b"],
             w_rg_a=w["w_rg_a"][0], w_rg_i=w["w_rg_i"][0], ln2_g=w["ln2_g"],
             lnf_g=w["lnf_g"].reshape(1, D))

    core = jnp.reshape(ci, (1,)).astype(jnp.int32)
    chip = jnp.reshape(2 * xi + yi, (1,)).astype(jnp.int32)
    first_sections = tuple(s for s in SECTIONS if s[0] in ("w_ff1_t", "w_ff2"))
    late_sections = SECTIONS[:1]
    early_sections = tuple(s for s in SECTIONS[1:] if s not in first_sections)
    in_flight = {}

    def pair_sum_and_send(group, sections, after):
        send_sems, recv_sems, sect, zones, _ = in_flight["pair_" + group]
        sect, got = _pair_exchange_wait(sections, send_sems, recv_sems, sect, zones, after,
                                        "grad_pair_exchange_wait_" + group)
        parts = _pair_add(sections, sect, got, core, "grad_pair_add_" + group)
        in_flight[group] = _chip_exchange_start(sections, parts, "grad_chip_exchange_start_" + group)
        return in_flight[group][-1]

    def pair_exchange_at_once(group, sections, grads, barrier_id):
        in_flight["pair_" + group] = _pair_exchange_start(
            sections, [grads[n] for n, _, _ in sections], barrier_id, "grad_pair_exchange_start_" + group)
        return pair_sum_and_send(group, sections, in_flight["pair_" + group][-1])

    def reduce_first(grads, after):
        if grads is None:
            return pair_sum_and_send("first", first_sections, after)
        in_flight["pair_first"] = _pair_exchange_start(
            first_sections, [grads[n] for n, _, _ in first_sections], PAIR_FIRST_ID,
            "grad_pair_exchange_start_first")
        return in_flight["pair_first"][-1]

    def reduce_early(grads):
        chan_g = jnp.concatenate([grads[n] for n, _ in CHAN], axis=0)
        chan_g = chan_g.reshape(N_CHAN_ROWS, N_DEV, LANES).transpose(1, 0, 2).astype(BF16)
        chan_g = jnp.pad(chan_g.reshape(N_DEV, -1), ((0, 0), (0, CHAN_BLOCK_ROWS * D - N_CHAN_ROWS * LANES)))
        grads["chan"] = chan_g.reshape(N_DEV * CHAN_BLOCK_ROWS, D)
        grads["gates"] = jnp.concatenate([grads[n].reshape(-1, D) for n in GATE_BLOCKS], axis=0).astype(BF16)
        return pair_exchange_at_once("early", early_sections, grads, PAIR_EARLY_ID)[0, 0]

    loss_part, grad_x, grads = _local_step(x[0], loss_target[0], p, late_weights, reduce_first, reduce_early)
    in_flight["pair_late"] = _pair_exchange_start(
        late_sections, [grads[n] for n, _, _ in late_sections], PAIR_LATE_ID, "grad_pair_exchange_start_late")

    arrived = {}

    def finish(group, sections, after, name):
        send_sems, recv_sems, parts, zones, _ = in_flight[group]
        parts, far = _chip_exchange_wait(sections, send_sems, recv_sems, parts, zones, after,
                                         "grad_chip_exchange_wait_" + name)
        keep = [i for i, s in enumerate(sections) if s[0] not in (*ROW_SHARDED, "w_in_t")]
        arrived.update({s[0]: (parts[i], far[i]) for i, s in enumerate(sections) if i not in keep})
        if not keep:
            return {}
        return dict(zip((sections[i][0] for i in keep),
                        _grad_finish([sections[i] for i in keep], [parts[i] for i in keep],
                                     [far[i] for i in keep], chip, "grad_finish_" + name)))

    summed = finish("first", first_sections, in_flight["pair_late"][-1], "first")
    summed.update(finish("early", early_sections, summed["w_ff1_t"], "early"))
    started_late = pair_sum_and_send("late", late_sections, summed["gates"])

    flat = jnp.concatenate([grads[n].reshape(-1) for n, _ in REPLICATED if n not in GATE_BLOCKS]
                           + [loss_part.reshape(-1) + started_late[0, 0]])
    n_small = flat.shape[0]
    flat = jnp.pad(flat, (0, SMALL_ROWS * LANES - n_small)).reshape(SMALL_ROWS, LANES)
    small_parts, gate_sum, _ = _all_gather([flat, summed["gates"]], "small_grad_all_gather")
    small = _sum_devices(small_parts, SMALL_ROWS, "small_grad_sum").reshape(-1)
    loss = small[n_small - 1]

    g, delta, new_m, new_v = {}, {}, {}, {}

    def update(n, g2, shape2):
        d2, m2, v2 = _adamw(w[n].reshape(shape2), g2, m[n].reshape(shape2), v[n].reshape(shape2),
                            "adamw_" + n)
        g[n], delta[n], new_m[n], new_v[n] = (a.reshape(w[n].shape) for a in (g2, d2, m2, v2))

    small_params = []
    o = 0
    for n, shape2 in REPLICATED:
        if n in GATE_BLOCKS:
            k, rows = GATE_BLOCKS.index(n), gate_sum.shape[0] // len(GATE_BLOCKS)
            update(n, gate_sum[k * rows:(k + 1) * rows].reshape(shape2), shape2)
        else:
            size = shape2[0] * shape2[1]
            small_params.append((n, small[o:o + size].reshape(shape2), shape2))
            o += size
    chan_back = summed["chan"].reshape(-1)[:N_CHAN_ROWS * LANES].reshape(N_CHAN_ROWS, LANES)
    r0 = 0
    for n, rows in CHAN:
        small_params.append((n, chan_back[r0:r0 + rows], (rows, LANES)))
        r0 += rows
    results = _adamw_small([(w[n].reshape(s2), g2, m[n].reshape(s2), v[n].reshape(s2))
                            for n, g2, s2 in small_params], "adamw_vectors")
    for (n, g2, _), (d2, m2, v2) in zip(small_params, results):
        g[n], delta[n], new_m[n], new_v[n] = (a.reshape(w[n].shape) for a in (g2, d2, m2, v2))

    for n in ROW_SHARDED:
        results = _adamw_from_parts(w[n][0], m[n][0], v[n][0], *arrived[n], chip, "adamw_" + n)
        g[n], delta[n], new_m[n], new_v[n] = (a[None] for a in results)
    for n, t in TRANSPOSED.items():
        if t in summed:
            update(n, summed[t].T, summed[t].shape[::-1])
    finish("late", late_sections, _after_all(list(delta.values()), "updates_done"), "late")
    results = _adamw_from_parts(w["w_in"][0].T, m["w_in"][0].T, v["w_in"][0].T, *arrived["w_in_t"], chip,
                                "adamw_w_in")
    g["w_in"], delta["w_in"], new_m["w_in"], new_v["w_in"] = (a.T[None] for a in results)

    return (loss, grad_x[None], *[g[n] for n in NAMES], *[delta[n] for n in NAMES],
            *[new_m[n] for n in NAMES], *[new_v[n] for n in NAMES])
```

```python
import math

import numpy as np
import jax
import jax.numpy as jnp
from jax import lax
from jax.experimental import pallas as pl
from jax.experimental.pallas import tpu as pltpu

F32 = jnp.float32
BF16 = jnp.bfloat16

T = 2048
D = 1024
D_ATT = 512
D_REC = 1024
D_FF = 4096
D_IN = 5632
N_HEADS = 8
DH = 64
GRID_W = 64
ROWS = T // GRID_W
WIN_H = 8
WIN_W = 16
KWIN = WIN_H * GRID_W
N_RPB_R = 2 * WIN_H - 1
N_RPB_C = 2 * WIN_W - 1
N_REC_BLOCKS = 16
REC_BLOCK = 64
CG = 128
N_CG = D_REC // CG
LRU_C = 8.0
EPS = 1e-6
N_DEV = 8
N_CHIPS = 4
LANES = 128

ADAM_LR = 0.001
ADAM_B1 = 0.9
ADAM_B2 = 0.999
ADAM_EPS = 1e-08
ADAM_WD = 0.01
ADAM_STEP = 10

MESH_AXES = ("x", "y", "c")
VMEM_LIMIT = 56 * 1024 * 1024

TILE = 512
DZ_ARRAYS = ((0, 3, 1), (3, 4, 2), (7, 4, 2))
N_DZ_TILES = D_IN // TILE


def _params(**kw):
    return pltpu.CompilerParams(vmem_limit_bytes=VMEM_LIMIT, **kw)


HG = 4
HQ = HG * GRID_W
HC = HG * DH


def _att_tables():
    rq = np.arange(GRID_W)
    kc = np.arange(KWIN) % GRID_W
    win_start = np.clip(rq - WIN_W // 2, 0, GRID_W - WIN_W)
    valid = (kc[None, :] >= win_start[:, None]) & (kc[None, :] < win_start[:, None] + WIN_W)
    same_head = (np.arange(HQ)[:, None] // GRID_W) == (np.arange(HC)[None, :] // DH)
    return valid.astype(np.float32), same_head.astype(np.float32)


def _pair_mask():
    half = np.arange(2 * DH) // DH
    return (half[:, None] == half[None, :]).astype(np.float32)


def _dup_table():
    return np.concatenate([np.eye(REC_BLOCK, dtype=np.float32)] * 2, axis=1)


def _sigmoid(x):
    return 0.5 * jnp.tanh(0.5 * x) + 0.5


def _softplus(x):
    return jnp.maximum(x, 0.0) + jnp.log(1.0 + jnp.exp(-jnp.abs(x)))


def _one_minus_square(log_a, a):
    x = 2.0 * log_a
    series = -x * (1.0 + x * (0.5 + x * (1.0 / 6.0)))
    return jnp.where(x > -0.02, series, 1.0 - a * a)


_GELU_C = math.sqrt(2.0 / math.pi)


def _gelu_and_grad(x):
    x2 = x * x
    inner = _GELU_C * (x + 0.044715 * x * x2)
    t = jnp.tanh(inner)
    g = 0.5 * x * (1.0 + t)
    dg = 0.5 * (1.0 + t) + 0.5 * x * (1.0 - t * t) * _GELU_C * (1.0 + 3.0 * 0.044715 * x2)
    return g, dg


def _dot(a, b):
    return jnp.dot(a, b, preferred_element_type=F32)


def _dot_nt(a, b):
    return lax.dot_general(a, b, (((1,), (1,)), ((), ())), preferred_element_type=F32)


def _dot_tn(a, b):
    return lax.dot_general(a, b, (((0,), (0,)), ((), ())), preferred_element_type=F32)


def _dot_exact(a, b):
    return jnp.dot(a, b, precision=lax.Precision.HIGHEST, preferred_element_type=F32)


def _shift_rows(x, s):
    n = x.shape[0]
    rows = lax.broadcasted_iota(jnp.int32, x.shape, 0)
    y = pltpu.roll(x, s % n, 0)
    if s > 0:
        return jnp.where(rows >= s, y, 0.0)
    return jnp.where(rows < n + s, y, 0.0)


def _rms_bwd(dh, xh, r, g):
    dxh = dh * g
    return r * (dxh - xh * jnp.mean(dxh * xh, axis=-1, keepdims=True))


def _matmul(a, b, mode, out_dtype, name, tm=512, tn=1024, tk=2048):
    if mode == "nn":
        (m, k), (k2, n) = a.shape, b.shape
    elif mode == "nt":
        (m, k), (n, k2) = a.shape, b.shape
    else:
        (k, m), (k2, n) = a.shape, b.shape
    assert k == k2
    tm, tn, tk = min(tm, m), min(tn, n), min(tk, k)
    assert m % tm == 0 and n % tn == 0 and k % tk == 0
    nk = k // tk
    dot = {"nn": _dot, "nt": _dot_nt, "tn": _dot_tn}[mode]

    def body(a_ref, b_ref, o_ref, acc):
        kk = pl.program_id(2)
        part = dot(a_ref[...].astype(BF16), b_ref[...].astype(BF16))
        if nk == 1:
            o_ref[...] = part.astype(out_dtype)
            return

        @pl.when(kk == 0)
        def _():
            acc[...] = part

        @pl.when(kk > 0)
        def _():
            acc[...] += part

        @pl.when(kk == nk - 1)
        def _():
            o_ref[...] = acc[...].astype(out_dtype)

    if mode == "tn":
        a_spec = pl.BlockSpec((tk, tm), lambda i, j, kk: (kk, i))
    else:
        a_spec = pl.BlockSpec((tm, tk), lambda i, j, kk: (i, kk))
    if mode == "nt":
        b_spec = pl.BlockSpec((tn, tk), lambda i, j, kk: (j, kk))
    else:
        b_spec = pl.BlockSpec((tk, tn), lambda i, j, kk: (kk, j))
    return pl.pallas_call(
        body, name=name,
        out_shape=jax.ShapeDtypeStruct((m, n), out_dtype),
        grid=(m // tm, n // tn, nk),
        in_specs=[a_spec, b_spec],
        out_specs=pl.BlockSpec((tm, tn), lambda i, j, kk: (i, j)),
        scratch_shapes=[pltpu.VMEM((tm, tn) if nk > 1 else (8, LANES), F32)],
        compiler_params=_params(dimension_semantics=("parallel", "parallel", "arbitrary")),
    )(a, b)


def _in_proj(x, g1, w_in_t, b_in):
    tm = 512

    def body(x_ref, g_ref, w_hbm, b_ref, qkv_ref, uy_ref, gg_ref, h_ref, w):
        @pl.when(pl.program_id(0) == 0)
        def _():
            pltpu.sync_copy(w_hbm, w)

        xv = x_ref[...]
        r = lax.rsqrt(jnp.mean(xv * xv, axis=-1, keepdims=True) + EPS)
        h = ((xv * r) * g_ref[...]).astype(BF16)
        h_ref[...] = h
        row0 = 0
        for ref in (qkv_ref, uy_ref, gg_ref):
            for c0 in range(0, ref.shape[1], TILE):
                z = _dot_nt(h, w[row0:row0 + TILE, :]) + b_ref[:, row0:row0 + TILE]
                ref[:, c0:c0 + TILE] = z.astype(ref.dtype)
                row0 += TILE

    tok = lambda width: pl.BlockSpec((tm, width), lambda i: (i, 0))
    return pl.pallas_call(
        body, name="in_proj",
        out_shape=(jax.ShapeDtypeStruct((T, 3 * D_ATT), BF16),
                   jax.ShapeDtypeStruct((T, 2 * D_REC), F32),
                   jax.ShapeDtypeStruct((T, 2 * D), F32),
                   jax.ShapeDtypeStruct((T, D), BF16)),
        grid=(T // tm,),
        in_specs=[tok(D), pl.BlockSpec((1, D), lambda i: (0, 0)), pl.BlockSpec(memory_space=pl.ANY),
                  pl.BlockSpec((1, D_IN), lambda i: (0, 0))],
        out_specs=(tok(3 * D_ATT), tok(2 * D_REC), tok(2 * D), tok(D)),
        scratch_shapes=[pltpu.VMEM((D_IN, D), BF16)],
        compiler_params=_params(dimension_semantics=("arbitrary",)),
    )(x, g1, w_in_t, b_in)


def _dz_specs(rows, tile_of, row_of):
    def spec(off, n, per_plane):
        def index(*ids):
            t = jnp.clip(tile_of(*ids) - off, 0, n - 1)
            return (t // per_plane, row_of(*ids), t % per_plane)
        return pl.BlockSpec((1, rows, TILE), index)
    return [spec(off, n, per) for off, n, per in DZ_ARRAYS]


def _dh_norm1_bwd(dz, w_in_t, x, g1, dx1):
    tm = 512

    def body(dqkv_ref, duy_ref, dgg_ref, w_hbm, x_ref, g_ref, dx1_ref, gx_ref, dg_ref, w):
        @pl.when(pl.program_id(0) == 0)
        def _():
            pltpu.sync_copy(w_hbm, w)
            dg_ref[...] = jnp.zeros_like(dg_ref)

        dh, row0 = None, 0
        for ref in (dqkv_ref, duy_ref, dgg_ref):
            for plane in range(ref.shape[0]):
                cols = ref.shape[2]
                part = _dot(ref[plane], w[row0:row0 + cols, :])
                dh = part if dh is None else dh + part
                row0 += cols
        xv = x_ref[...]
        r = lax.rsqrt(jnp.mean(xv * xv, axis=-1, keepdims=True) + EPS)
        xh = xv * r
        dg_ref[...] += jnp.sum(dh * xh, axis=0, keepdims=True)
        gx_ref[...] = dx1_ref[...] + _rms_bwd(dh, xh, r, g_ref[...])

    tok = pl.BlockSpec((tm, D), lambda i: (i, 0))
    vec = pl.BlockSpec((1, D), lambda i: (0, 0))
    planes = lambda a: pl.BlockSpec((a.shape[0], tm, a.shape[2]), lambda i: (0, i, 0))
    return pl.pallas_call(
        body, name="dh_norm1_bwd",
        out_shape=(jax.ShapeDtypeStruct((T, D), F32), jax.ShapeDtypeStruct((1, D), F32)),
        grid=(T // tm,),
        in_specs=[planes(a) for a in dz] + [pl.BlockSpec(memory_space=pl.ANY), tok, vec, tok],
        out_specs=(tok, vec),
        scratch_shapes=[pltpu.VMEM((D_IN, D), BF16)],
        compiler_params=_params(dimension_semantics=("arbitrary",)),
    )(*dz, w_in_t, x, g1, dx1)


def _grad_w_in(dz, h):
    def body(*refs):
        seg_refs = refs[:3]
        h_ref, gw_ref, gb_ref = refs[3:]
        j = pl.program_id(0)

        for s, (off, n, _) in enumerate(DZ_ARRAYS):
            @pl.when((j >= off) & (j < off + n))
            def _(s=s):
                a = seg_refs[s][0]
                gw_ref[...] = _dot_tn(a, h_ref[...]).astype(BF16)
                gb_ref[...] = jnp.sum(a.astype(F32), axis=0, keepdims=True)

    return pl.pallas_call(
        body, name="grad_w_in",
        out_shape=(jax.ShapeDtypeStruct((D_IN, D), BF16), jax.ShapeDtypeStruct((1, D_IN), F32)),
        grid=(N_DZ_TILES,),
        in_specs=_dz_specs(T, lambda j: j, lambda j: 0) + [pl.BlockSpec((T, D), lambda j: (0, 0))],
        out_specs=(pl.BlockSpec((TILE, D), lambda j: (j, 0)), pl.BlockSpec((1, TILE), lambda j: (0, j))),
        compiler_params=_params(dimension_semantics=("parallel",)),
    )(*dz, h)


def _rpb_rows(rpb):
    padded = jnp.pad(rpb, ((0, 0), (0, 0), (0, GRID_W - N_RPB_C)))
    rows = [padded[:, WIN_H - 1 - oi: 2 * WIN_H - 1 - oi].reshape(N_HEADS // HG, HG, KWIN)
            for oi in range(WIN_H)]
    return jnp.stack(rows, axis=0)


SKEW = KWIN - (WIN_W - 1)


MASKED = -1e30


def _bias_tiles(rows_ref, valid, bias_s):
    for oi in range(WIN_H):
        for hh in range(HG):
            row = jnp.broadcast_to(rows_ref[oi, 0, hh:hh + 1, :], (GRID_W, KWIN))
            tile = pltpu.roll(row, SKEW, 1, stride=1, stride_axis=0)
            bias_s[oi, hh * GRID_W:(hh + 1) * GRID_W, :] = jnp.where(valid, tile, MASKED)


def _bias_tile_grads(gb_s, flip, out_ref):
    for oi in range(WIN_H):
        for hh in range(HG):
            g = _dot_exact(flip, gb_s[oi, hh * GRID_W:(hh + 1) * GRID_W, :])
            back = pltpu.roll(g, KWIN - (GRID_W - WIN_W), 1, stride=1, stride_axis=0)
            out_ref[0, oi, hh:hh + 1, :] = jnp.sum(back, axis=0, keepdims=True)


def _rpb_fold(row_grads):
    g = row_grads.transpose(1, 0, 2, 3).reshape(WIN_H, N_HEADS, WIN_H, GRID_W)
    g = g.transpose(0, 2, 1, 3)

    def body(g_ref, o_ref):
        for dr in range(N_RPB_R):
            terms = [g_ref[oi, i] for oi in range(WIN_H) for i in range(WIN_H) if i - oi + WIN_H - 1 == dr]
            acc = terms[0]
            for term in terms[1:]:
                acc = acc + term
            o_ref[dr] = acc

    out = pl.pallas_call(
        body, name="rpb_fold",
        out_shape=jax.ShapeDtypeStruct((N_RPB_R, N_HEADS, GRID_W), F32),
    )(g)
    return out.transpose(1, 0, 2)[:, :, :N_RPB_C]


ATT_GROUPS = N_HEADS // HG
ATT_UNROLL = 8


def _stacked(rows64, same_head):
    return jnp.where(same_head, jnp.concatenate([rows64] * HG, axis=0), jnp.zeros((), BF16))


def _own_heads(stacked):
    head = lax.broadcasted_iota(jnp.int32, (GRID_W, HC), 1) // DH
    out = stacked[:GRID_W]
    for h in range(1, HG):
        out = jnp.where(head == h, stacked[h * GRID_W:(h + 1) * GRID_W], out)
    return out


def _att_scores(q_ref, k_ref, bias_ref, same_head, r):
    rs = jnp.clip(r - WIN_H // 2, 0, ROWS - WIN_H)
    oi = r - rs
    q0 = pl.multiple_of(r * GRID_W, GRID_W)
    k0 = pl.multiple_of(rs * GRID_W, GRID_W)
    q2 = _stacked(q_ref[pl.ds(q0, GRID_W), :] * (DH ** -0.5), same_head)
    kw = k_ref[pl.ds(k0, KWIN), :]
    s = _dot_nt(q2, kw) + bias_ref[oi]
    e = jnp.exp(s - jnp.max(s, axis=-1, keepdims=True))
    return e, 1.0 / jnp.sum(e, axis=-1, keepdims=True), q2, kw, q0, k0, oi


def _att_specs():
    col = lambda off: pl.BlockSpec((T, HC), lambda g: (0, g + off * ATT_GROUPS))
    tables = [pl.BlockSpec((WIN_H, 1, HG, KWIN), lambda g: (0, g, 0, 0)),
              pl.BlockSpec((GRID_W, KWIN), lambda g: (0, 0)),
              pl.BlockSpec((HQ, HC), lambda g: (0, 0))]
    return col, tables, pltpu.VMEM((WIN_H, HQ, KWIN), F32)


def _att_fwd(qkv, bias_rows):
    valid_np, same_head_np = _att_tables()

    def body(q_ref, k_ref, v_ref, rows_ref, valid_ref, head_ref, o_ref, bias_s):
        same_head = head_ref[...] > 0.5
        _bias_tiles(rows_ref, valid_ref[...] > 0.5, bias_s)

        def row(r, carry):
            e, rl, _, _, q0, k0, _ = _att_scores(q_ref, k_ref, bias_s, same_head, r)
            o2 = _dot((e * rl).astype(BF16), v_ref[pl.ds(k0, KWIN), :])
            o_ref[pl.ds(q0, GRID_W), :] = _own_heads(o2).astype(BF16)
            return carry

        lax.fori_loop(0, ROWS, row, 0, unroll=ATT_UNROLL)

    col, tables, tiles = _att_specs()
    return pl.pallas_call(
        body, name="att_fwd",
        out_shape=jax.ShapeDtypeStruct((T, D_ATT), BF16),
        grid=(ATT_GROUPS,),
        in_specs=[col(0), col(1), col(2)] + tables,
        out_specs=col(0),
        scratch_shapes=[tiles],
        compiler_params=_params(dimension_semantics=("parallel",)),
    )(qkv, qkv, qkv, bias_rows, jnp.asarray(valid_np), jnp.asarray(same_head_np))


def _att_bwd(qkv, bias_rows, datt, after):
    valid_np, same_head_np = _att_tables()

    def body(q_ref, k_ref, v_ref, do_ref, rows_ref, valid_ref, head_ref, flip_ref,
             dqkv_ref, grows_ref, dk_acc, dv_acc, bias_s, gb_s):
        same_head = head_ref[...] > 0.5
        dk_acc[...] = jnp.zeros_like(dk_acc)
        dv_acc[...] = jnp.zeros_like(dv_acc)
        gb_s[...] = jnp.zeros_like(gb_s)
        _bias_tiles(rows_ref, valid_ref[...] > 0.5, bias_s)

        def row(r, carry):
            e, rl, q2, kw, q0, k0, oi = _att_scores(q_ref, k_ref, bias_s, same_head, r)
            do2 = _stacked(do_ref[pl.ds(q0, GRID_W), :], same_head)
            vw = v_ref[pl.ds(k0, KWIN), :]
            p = e * rl
            dp = _dot_nt(do2, vw)
            ds = p * (dp - jnp.sum(dp * p, axis=-1, keepdims=True))
            p16 = p.astype(BF16)
            ds16 = ds.astype(BF16)
            dv_acc[pl.ds(k0, KWIN), :] += _dot_tn(p16, do2)
            dk_acc[pl.ds(k0, KWIN), :] += _dot_tn(ds16, q2)
            dq2 = _dot(ds16, kw) * (DH ** -0.5)
            dqkv_ref[0, pl.ds(q0, GRID_W), :] = _own_heads(dq2).astype(BF16)
            gb_s[oi] += ds
            return carry

        lax.fori_loop(0, ROWS, row, 0, unroll=ATT_UNROLL)
        dqkv_ref[1] = dk_acc[...].astype(BF16)
        dqkv_ref[2] = dv_acc[...].astype(BF16)
        _bias_tile_grads(gb_s, flip_ref[...], grows_ref)

    col, tables, tiles = _att_specs()
    return pl.pallas_call(
        body, name="att_bwd",
        out_shape=(jax.ShapeDtypeStruct((3, T, D_ATT), BF16),
                   jax.ShapeDtypeStruct((ATT_GROUPS, WIN_H, HG, KWIN), F32)),
        grid=(ATT_GROUPS,),
        in_specs=[col(0), col(1), col(2), col(0)] + tables + [pl.BlockSpec((GRID_W, GRID_W), lambda g: (0, 0))],
        out_specs=(pl.BlockSpec((3, T, HC), lambda g: (0, 0, g)),
                   pl.BlockSpec((1, WIN_H, HG, KWIN), lambda g: (g, 0, 0, 0))),
        scratch_shapes=[pltpu.VMEM((T, HC), F32), pltpu.VMEM((T, HC), F32), tiles, tiles],
        compiler_params=_params(dimension_semantics=("parallel",)),
    )(qkv, qkv, qkv, datt, bias_rows, jnp.asarray(valid_np) + after, jnp.asarray(same_head_np),
      jnp.asarray(np.eye(GRID_W, dtype=np.float32)[::-1].copy()))


def _conv_taps(up):
    return (_shift_rows(up, 2), _shift_rows(up, 1), up, _shift_rows(up, -1))


def _pair_block_diag(w_pair, dup, same_half):
    return jnp.where(same_half, _dot(w_pair.astype(BF16), dup), 0.0).astype(BF16)


def _gates(u, u16, wa, ba, wi, bi, lam):
    r = _sigmoid(_dot(u16, wa) + ba)
    ig = _sigmoid(_dot(u16, wi) + bi)
    sp = _softplus(-lam)
    log_a = (-LRU_C) * r * sp
    a = jnp.exp(log_a)
    mult2 = jnp.maximum(_one_minus_square(log_a, a), 0.0)
    return r, ig, sp, a, jnp.sqrt(mult2), mult2


SCAN_BLOCKS = 8


def _scans(jobs):
    c = jobs[0][0].shape[1]
    nblk = T // 8
    rows = lax.broadcasted_iota(jnp.int32, (8, c), 0)

    def block(a, b, reverse):
        for s in (1, 2, 4):
            if reverse:
                keep = rows < 8 - s
                a_s = jnp.where(keep, pltpu.roll(a, 8 - s, 0), 1.0)
                b_s = jnp.where(keep, pltpu.roll(b, 8 - s, 0), 0.0)
            else:
                keep = rows >= s
                a_s = jnp.where(keep, pltpu.roll(a, s, 0), 1.0)
                b_s = jnp.where(keep, pltpu.roll(b, s, 0), 0.0)
            b = a * b_s + b
            a = a * a_s
        return a, b

    def step(i, carry):
        out = []
        for (a_ref, b_ref, h_ref, reverse), h_prev in zip(jobs, carry):
            for u in range(SCAN_BLOCKS):
                blk = i * SCAN_BLOCKS + u
                if reverse:
                    blk = nblk - 1 - blk
                t0 = pl.multiple_of(blk * 8, 8)
                a, b = block(a_ref[pl.ds(t0, 8), :], b_ref[pl.ds(t0, 8), :], reverse)
                h = a * h_prev + b
                h_ref[pl.ds(t0, 8), :] = h
                h_prev = jnp.broadcast_to(h[0:1] if reverse else h[7:8], (8, c))
            out.append(h_prev)
        return tuple(out)

    lax.fori_loop(0, nblk // SCAN_BLOCKS, step, tuple(jnp.zeros((8, c), F32) for _ in jobs))


def _rec_specs():
    tok = lambda off: pl.BlockSpec((T, CG), lambda g: (0, g + off))
    per_ch = lambda rows: pl.BlockSpec((rows, CG), lambda g: (0, g))
    wspec = pl.BlockSpec((2, 1, CG, REC_BLOCK), lambda g: (0, g, 0, 0))
    const = lambda shape: pl.BlockSpec(shape, lambda g: (0, 0))
    return tok, per_ch, wspec, const


def _rec_fwd(uy, conv_w, conv_b, w_a, b_a, w_i, b_i, lam):
    tok, per_ch, wspec, const = _rec_specs()

    def body(up_ref, yb_ref, cw_ref, cb_ref, wa_ref, ba_ref, wi_ref, bi_ref, lam_ref, dup_ref, half_ref,
             hf_ref, hb_ref, yrec_ref, am_ref, bx_f, bx_b):
        dup = dup_ref[...]
        same_half = half_ref[...] > 0.5
        taps = _conv_taps(up_ref[...])
        u = cb_ref[...]
        for j in range(4):
            u = u + taps[j] * cw_ref[j:j + 1, :]
        u16 = u.astype(BF16)
        for d, bx_s in enumerate((bx_f, bx_b)):
            wa = _pair_block_diag(wa_ref[d, 0], dup, same_half)
            wi = _pair_block_diag(wi_ref[d, 0], dup, same_half)
            _, ig, _, a, mult, _ = _gates(u, u16, wa, ba_ref[d:d + 1, :], wi, bi_ref[d:d + 1, :],
                                       lam_ref[d:d + 1, :])
            am_ref[2 * d] = a
            am_ref[2 * d + 1] = mult
            bx_s[...] = mult * (ig * u)
        _scans([(am_ref.at[0], bx_f, hf_ref, False), (am_ref.at[2], bx_b, hb_ref, True)])
        gelu, _ = _gelu_and_grad(yb_ref[...])
        yrec_ref[...] = ((hf_ref[...] + hb_ref[...]) * gelu).astype(BF16)

    return pl.pallas_call(
        body, name="rec_fwd",
        out_shape=(jax.ShapeDtypeStruct((T, D_REC), F32), jax.ShapeDtypeStruct((T, D_REC), F32),
                   jax.ShapeDtypeStruct((T, D_REC), BF16), jax.ShapeDtypeStruct((4, T, D_REC), F32)),
        grid=(N_CG,),
        in_specs=[tok(0), tok(N_CG), per_ch(4), per_ch(1), wspec, per_ch(2), wspec, per_ch(2), per_ch(2),
                  const((REC_BLOCK, CG)), const((CG, CG))],
        out_specs=(tok(0), tok(0), tok(0), pl.BlockSpec((4, T, CG), lambda g: (0, 0, g))),
        scratch_shapes=[pltpu.VMEM((T, CG), F32)] * 2,
        compiler_params=_params(dimension_semantics=("parallel",)),
    )(uy, uy, conv_w, conv_b, w_a, b_a, w_i, b_i, lam,
      jnp.asarray(_dup_table(), BF16), jnp.asarray(_pair_mask()))


def _rec_bwd(uy, hf, hb, am, dyrec, conv_w, conv_b, w_a, b_a, w_i, b_i, lam):
    tok, per_ch, wspec, const = _rec_specs()

    def body(up_ref, yb_ref, hf_ref, hb_ref, am_ref, dy_ref, cw_ref, cb_ref, wa_ref, ba_ref, wi_ref, bi_ref,
             lam_ref, dup_ref, dupt_ref, half_ref,
             duy_ref, dcw_ref, dcb_ref, dwa_ref, dba_ref, dwi_ref, dbi_ref, dlam_ref,
             a_s0, a_s1, dh_s, g_s0, g_s1):
        dup = dup_ref[...]
        dup_t = dupt_ref[...]
        same_half = half_ref[...] > 0.5
        taps = _conv_taps(up_ref[...])
        u = cb_ref[...]
        for j in range(4):
            u = u + taps[j] * cw_ref[j:j + 1, :]
        u16 = u.astype(BF16)
        gelu, dgelu = _gelu_and_grad(yb_ref[...])
        dy = dy_ref[...]
        duy_ref[1] = (dy * (hf_ref[...] + hb_ref[...]) * dgelu).astype(BF16)
        dh_s[...] = dy * gelu
        a_s0[...] = _shift_rows(am_ref[0], -1)
        a_s1[...] = _shift_rows(am_ref[2], 1)
        _scans([(a_s0, dh_s, g_s0, True), (a_s1, dh_s, g_s1, False)])
        du = jnp.zeros((T, CG), F32)
        for d, g_s in enumerate((g_s0, g_s1)):
            reverse = d == 1
            wa = _pair_block_diag(wa_ref[d, 0], dup, same_half)
            wi = _pair_block_diag(wi_ref[d, 0], dup, same_half)
            lam_d = lam_ref[d:d + 1, :]
            r = _sigmoid(_dot(u16, wa) + ba_ref[d:d + 1, :])
            ig = _sigmoid(_dot(u16, wi) + bi_ref[d:d + 1, :])
            sp = _softplus(-lam_d)
            a, mult = am_ref[2 * d], am_ref[2 * d + 1]
            mult2 = mult * mult
            g = g_s[...]
            h_prev = _shift_rows(hb_ref[...], -1) if reverse else _shift_rows(hf_ref[...], 1)
            da = g * h_prev
            dmult = g * (ig * u)
            dig = g * mult * u
            du = du + g * mult * ig
            dmult_dlog = jnp.where(mult2 > 0.0, -(a * a) * lax.rsqrt(mult2), 0.0)
            dlog_a = da * a + dmult * dmult_dlog
            dr = dlog_a * ((-LRU_C) * sp)
            dsp = jnp.sum(dlog_a * ((-LRU_C) * r), axis=0, keepdims=True)
            dlam_ref[d:d + 1, :] = dsp * (-_sigmoid(-lam_d))
            dga = dr * r * (1.0 - r)
            dgi = dig * ig * (1.0 - ig)
            dga16 = dga.astype(BF16)
            dgi16 = dgi.astype(BF16)
            du = du + _dot_nt(dga16, wa) + _dot_nt(dgi16, wi)
            dwa_ref[d, 0] = _dot_exact(jnp.where(same_half, _dot_tn(u16, dga16), 0.0), dup_t)
            dwi_ref[d, 0] = _dot_exact(jnp.where(same_half, _dot_tn(u16, dgi16), 0.0), dup_t)
            dba_ref[d:d + 1, :] = jnp.sum(dga, axis=0, keepdims=True)
            dbi_ref[d:d + 1, :] = jnp.sum(dgi, axis=0, keepdims=True)
        dcb_ref[...] = jnp.sum(du, axis=0, keepdims=True)
        for j in range(4):
            dcw_ref[j:j + 1, :] = jnp.sum(du * taps[j], axis=0, keepdims=True)
        dup_in = (_shift_rows(du, -2) * cw_ref[0:1, :] + _shift_rows(du, -1) * cw_ref[1:2, :]
                  + du * cw_ref[2:3, :] + _shift_rows(du, 1) * cw_ref[3:4, :])
        duy_ref[0] = dup_in.astype(BF16)

    wshape = jax.ShapeDtypeStruct((2, N_CG, CG, REC_BLOCK), F32)
    vec = lambda rows: jax.ShapeDtypeStruct((rows, D_REC), F32)
    dup_np = _dup_table()
    return pl.pallas_call(
        body, name="rec_bwd",
        out_shape=(jax.ShapeDtypeStruct((2, T, D_REC), BF16),
                   vec(4), vec(1), wshape, vec(2), wshape, vec(2), vec(2)),
        grid=(N_CG,),
        in_specs=[tok(0), tok(N_CG), tok(0), tok(0), pl.BlockSpec((4, T, CG), lambda g: (0, 0, g)), tok(0),
                  per_ch(4), per_ch(1), wspec, per_ch(2), wspec, per_ch(2), per_ch(2),
                  const((REC_BLOCK, CG)), const((CG, REC_BLOCK)), const((CG, CG))],
        out_specs=(pl.BlockSpec((2, T, CG), lambda g: (0, 0, g)),
                   per_ch(4), per_ch(1), wspec, per_ch(2), wspec, per_ch(2), per_ch(2)),
        scratch_shapes=[pltpu.VMEM((T, CG), F32)] * 5,
        compiler_params=_params(dimension_semantics=("parallel",)),
    )(uy, uy, hf, hb, am, dyrec, conv_w, conv_b, w_a, b_a, w_i, b_i, lam,
      jnp.asarray(dup_np, BF16), jnp.asarray(dup_np.T.copy()), jnp.asarray(_pair_mask()))


TM_MIX = 256


def _mix_specs():
    tok = lambda width, blk=0: pl.BlockSpec((TM_MIX, width), lambda i: (i, blk))
    full = lambda shape: pl.BlockSpec(shape, lambda i: (0, 0))
    return tok, full


def _mix_fwd(x, att, yrec, gg, w_att_o_t, w_rec_o, w_out):
    tok, full = _mix_specs()

    def body(x_ref, att_ref, yr_ref, ga_ref, gr_ref, wao_ref, wro_ref, wo_ref, x1_ref, mixed_ref):
        y_att = _dot_nt(att_ref[...], wao_ref[...])
        y_rec = _dot(yr_ref[...], wro_ref[...])
        mixed = (_sigmoid(ga_ref[...]) * y_att + _sigmoid(gr_ref[...]) * y_rec).astype(BF16)
        mixed_ref[...] = mixed
        x1_ref[...] = x_ref[...] + _dot(mixed, wo_ref[...])

    return pl.pallas_call(
        body, name="mix_fwd",
        out_shape=(jax.ShapeDtypeStruct((T, D), F32), jax.ShapeDtypeStruct((T, D), BF16)),
        grid=(T // TM_MIX,),
        in_specs=[tok(D), tok(D_ATT), tok(D_REC), tok(D, 0), tok(D, 1),
                  full((D, D_ATT)), full((D_REC, D)), full((D, D))],
        out_specs=(tok(D), tok(D)),
        compiler_params=_params(dimension_semantics=("parallel",)),
    )(x, att, yrec, gg, gg, w_att_o_t, w_rec_o, w_out)


def _mix_bwd(dx1, att, yrec, gg, w_att_o_t, w_rec_o, w_out, after):
    tok, full = _mix_specs()

    def body(dx_ref, att_ref, yr_ref, ga_ref, gr_ref, wao_ref, wro_ref, wo_ref, after_ref,
             dgg_ref, dya_ref, dyr_ref, datt_ref, dyrp_ref):
        dmixed = _dot_nt(dx_ref[...].astype(BF16), wo_ref[...])
        y_att = _dot_nt(att_ref[...], wao_ref[...])
        y_rec = _dot(yr_ref[...], wro_ref[...])
        sa = _sigmoid(ga_ref[...])
        sr = _sigmoid(gr_ref[...])
        dgg_ref[0] = (dmixed * y_att * sa * (1.0 - sa)).astype(BF16)
        dgg_ref[1] = (dmixed * y_rec * sr * (1.0 - sr)).astype(BF16)
        dya = (dmixed * sa).astype(BF16)
        dyr = (dmixed * sr).astype(BF16)
        dya_ref[...] = dya
        dyr_ref[...] = dyr
        datt_ref[...] = _dot(dya, wao_ref[...]).astype(BF16)
        dyrp_ref[...] = _dot_nt(dyr, wro_ref[...])

    return pl.pallas_call(
        body, name="mix_bwd",
        out_shape=(jax.ShapeDtypeStruct((2, T, D), BF16),
                   jax.ShapeDtypeStruct((T, D), BF16), jax.ShapeDtypeStruct((T, D), BF16),
                   jax.ShapeDtypeStruct((T, D_ATT), BF16), jax.ShapeDtypeStruct((T, D_REC), F32)),
        grid=(T // TM_MIX,),
        in_specs=[tok(D), tok(D_ATT), tok(D_REC), tok(D, 0), tok(D, 1),
                  full((D, D_ATT)), full((D_REC, D)), full((D, D)), pl.BlockSpec(memory_space=pl.ANY)],
        out_specs=(pl.BlockSpec((2, TM_MIX, D), lambda i: (0, i, 0)),
                   tok(D), tok(D), tok(D_ATT), tok(D_REC)),
        compiler_params=_params(dimension_semantics=("parallel",)),
    )(dx1, att, yrec, gg, gg, w_att_o_t, w_rec_o, w_out, after)


TM_FFN = 256
FF_CHUNK = 1024


def _ffn_loss(x1, target, g2, gf, w_ff1_t, w_ff2):
    n_chunks = D_FF // FF_CHUNK

    def body(x1_ref, tg_ref, g2_ref, gf_ref, w1_hbm, w2_hbm,
             loss_ref, dx1_ref, h2_ref, act_ref, dpre_ref, dx2_ref, dg2_ref, dgf_ref,
             w1, w2, relu_s):
        i = pl.program_id(0)

        @pl.when(i == 0)
        def _():
            pltpu.sync_copy(w1_hbm, w1)
            pltpu.sync_copy(w2_hbm, w2)
            loss_ref[...] = jnp.zeros_like(loss_ref)
            dg2_ref[...] = jnp.zeros_like(dg2_ref)
            dgf_ref[...] = jnp.zeros_like(dgf_ref)

        x1v = x1_ref[...]
        r2 = lax.rsqrt(jnp.mean(x1v * x1v, axis=-1, keepdims=True) + EPS)
        xh2 = x1v * r2
        h2 = (xh2 * g2_ref[...]).astype(BF16)
        h2_ref[...] = h2
        x2 = x1v
        for c in range(n_chunks):
            ff = slice(c * FF_CHUNK, (c + 1) * FF_CHUNK)
            rl = jnp.maximum(_dot_nt(h2, w1[ff, :]), 0.0)
            relu_s[:, ff] = rl
            act = (rl * rl).astype(BF16)
            act_ref[:, ff] = act
            x2 = x2 + _dot(act, w2[ff, :])
        r3 = lax.rsqrt(jnp.mean(x2 * x2, axis=-1, keepdims=True) + EPS)
        xh3 = x2 * r3
        err = xh3 * gf_ref[...] - tg_ref[...]
        loss_ref[...] += 0.5 * jnp.sum(jnp.mean(err * err, axis=-1, keepdims=True))
        dy = err * (1.0 / D)
        dgf_ref[...] += jnp.sum(dy * xh3, axis=0, keepdims=True)
        dx2 = _rms_bwd(dy, xh3, r3, gf_ref[...])
        dx2_16 = dx2.astype(BF16)
        dx2_ref[...] = dx2_16
        dh2 = jnp.zeros((TM_FFN, D), F32)
        for c in range(n_chunks):
            ff = slice(c * FF_CHUNK, (c + 1) * FF_CHUNK)
            dpre = (_dot_nt(dx2_16, w2[ff, :]) * (2.0 * relu_s[:, ff])).astype(BF16)
            dpre_ref[:, ff] = dpre
            dh2 = dh2 + _dot(dpre, w1[ff, :])
        dg2_ref[...] += jnp.sum(dh2 * xh2, axis=0, keepdims=True)
        dx1_ref[...] = dx2 + _rms_bwd(dh2, xh2, r2, g2_ref[...])

    tok = lambda width: pl.BlockSpec((TM_FFN, width), lambda i: (i, 0))
    vec = pl.BlockSpec((1, D), lambda i: (0, 0))
    hbm = pl.BlockSpec(memory_space=pl.ANY)
    return pl.pallas_call(
        body, name="ffn_loss",
        out_shape=(jax.ShapeDtypeStruct((8, 128), F32), jax.ShapeDtypeStruct((T, D), F32),
                   jax.ShapeDtypeStruct((T, D), BF16), jax.ShapeDtypeStruct((T, D_FF), BF16),
                   jax.ShapeDtypeStruct((T, D_FF), BF16), jax.ShapeDtypeStruct((T, D), BF16),
                   jax.ShapeDtypeStruct((1, D), F32), jax.ShapeDtypeStruct((1, D), F32)),
        grid=(T // TM_FFN,),
        in_specs=[tok(D), tok(D), vec, vec, hbm, hbm],
        out_specs=(pl.BlockSpec((8, 128), lambda i: (0, 0)), tok(D), tok(D), tok(D_FF), tok(D_FF), tok(D),
                   vec, vec),
        scratch_shapes=[pltpu.VMEM((D_FF, D), BF16), pltpu.VMEM((D_FF, D), BF16),
                        pltpu.VMEM((TM_FFN, D_FF), F32)],
        compiler_params=_params(dimension_semantics=("arbitrary",)),
    )(x1, target, g2, gf, w_ff1_t, w_ff2)


def _local_step(x, target, p, late_weights, reduce_first, reduce_early):
    bias = _rpb_rows(p["rpb"])
    pairs = lambda w: w.reshape(2, N_CG, CG, REC_BLOCK)
    w_a, w_i = pairs(p["w_rg_a"]), pairs(p["w_rg_i"])
    rec_params = (p["conv_w"], p["conv_b"], w_a, p["b_rg_a"], w_i, p["b_rg_i"], p["lru_lambda"])

    qkv, uy, gg, h = _in_proj(x, p["ln1_g"], p["w_in_t"], p["b_in"])
    att = _att_fwd(qkv, bias)
    hf, hb, yrec, am = _rec_fwd(uy, *rec_params)
    p = {**p, **late_weights(yrec, 0)}
    x1, mixed = _mix_fwd(x, att, yrec, gg, p["w_att_o_t"], p["w_rec_o"], p["w_out"])
    p = {**p, **late_weights(x1, 1)}
    loss8, dx1, h2, act, dpre, dx2, g_ln2, g_lnf = _ffn_loss(
        x1, target, p["ln2_g"], p["lnf_g"], p["w_ff1_t"], p["w_ff2"])

    grads = {"ln2_g": g_ln2, "lnf_g": g_lnf,
             "w_ff1_t": _matmul(dpre, h2, "tn", BF16, "g_w_ff1"),
             "w_ff2": _matmul(act, dx2, "tn", BF16, "g_w_ff2")}
    dgg, dya, dyr, datt, dyrp = _mix_bwd(dx1, att, yrec, gg, p["w_att_o_t"], p["w_rec_o"], p["w_out"],
                                         reduce_first(grads, None))
    lam_after = rec_params[-1] + reduce_first(None, dgg)[0, 0]
    duy, g_cw, g_cb, g_wa, g_ba, g_wi, g_bi, g_lam = _rec_bwd(uy, hf, hb, am, dyrp, *rec_params[:-1], lam_after)
    blocks = lambda g: g.reshape(2, N_REC_BLOCKS, REC_BLOCK, REC_BLOCK)
    grads.update({
        "w_att_o_t": _matmul(dya, att, "tn", BF16, "g_w_att_o"),
        "conv_w": g_cw, "conv_b": g_cb, "w_rg_a": blocks(g_wa), "b_rg_a": g_ba,
        "w_rg_i": blocks(g_wi), "b_rg_i": g_bi, "lru_lambda": g_lam,
        "w_rec_o": _matmul(yrec, dyr, "tn", BF16, "g_w_rec_o"),
        "w_out": _matmul(mixed, dx1, "tn", BF16, "g_w_out"),
    })
    dqkv, gbias = _att_bwd(qkv, bias, datt, reduce_early(grads))
    dz = (dqkv, duy, dgg)
    grad_x, g_ln1 = _dh_norm1_bwd(dz, p["w_in_t"], x, p["ln1_g"], dx1)
    g_w_in_t, g_b_in = _grad_w_in(dz, h)
    grads.update(ln1_g=g_ln1, w_in_t=g_w_in_t, b_in=g_b_in, rpb=_rpb_fold(gbias))
    return loss8[0:1, 0:1], grad_x, grads


MESH_ID = pl.DeviceIdType.MESH
ANY = pl.BlockSpec(memory_space=pl.ANY)

CHAN_BLOCK_ROWS = 32
GATE_ROWS = 2 * 2 * N_REC_BLOCKS * REC_BLOCK * REC_BLOCK // (N_DEV * D)
SECTIONS = (("w_in_t", 704, D), ("w_rec_o", 128, D), ("w_out", 128, D), ("w_ff1_t", 512, D),
            ("w_ff2", 512, D), ("chan", CHAN_BLOCK_ROWS, D), ("w_att_o_t", 128, D_ATT),
            ("gates", GATE_ROWS, D))
N_SEC = len(SECTIONS)
N_CHAN_ROWS = 10
CHAN = (("conv_w", 4), ("b_rg_a", 2), ("b_rg_i", 2), ("lru_lambda", 2))


def _position():
    return lax.axis_index("x"), lax.axis_index("y"), lax.axis_index("c")


def _other_chips(x, y):
    return [(1 - x, y), (x, 1 - y), (1 - x, 1 - y)]


PASS_ON_IDS, PAIR_EARLY_ID, PAIR_LATE_ID, PAIR_FIRST_ID = (1, 4), 2, 3, 5


def _pair_handshake(x, y, c):
    barrier = pltpu.get_barrier_semaphore()
    pl.semaphore_signal(barrier, inc=1, device_id=(x, y, 1 - c), device_id_type=MESH_ID)
    pl.semaphore_wait(barrier, 1)


def _block_of(ref, dev, rows):
    return ref.at[pl.ds(pl.multiple_of(dev * rows, 16), rows)]


def _all_gather(shards, name):
    ns = len(shards)

    def body(*refs):
        x_refs, out_refs, done_ref = refs[:ns], refs[ns:2 * ns], refs[2 * ns]
        send_sems, recv_sems, local_sems = refs[2 * ns + 1:]
        done_ref[0, 0] = 0.0
        x, y, c = _position()
        me, sibling = (x, y, c), (x, y, 1 - c)
        x_nbr, y_nbr, diagonal = _other_chips(x, y)
        north = c == 1
        relay_from = (jnp.where(north, x_nbr[0], y_nbr[0]), jnp.where(north, x_nbr[1], y_nbr[1]))
        relay_to = (jnp.where(north, y_nbr[0], x_nbr[0]), jnp.where(north, y_nbr[1], x_nbr[1]))

        def rows(s, px, py, pc):
            return _block_of(out_refs[s], 4 * px + 2 * py + pc, shards[s].shape[0])

        def copy(k, s, block, to, from_shard=False):
            return pltpu.make_async_remote_copy(
                src_ref=x_refs[s] if from_shard else rows(s, *block), dst_ref=rows(s, *block),
                send_sem=send_sems.at[k * ns + s], recv_sem=recv_sems.at[k * ns + s],
                device_id=to, device_id_type=MESH_ID)

        sections = range(ns)
        mine = [pltpu.make_async_copy(x_refs[s], rows(s, *me), local_sems.at[s]) for s in sections]
        sent = [copy(k, s, me, to, True) for k, to in enumerate((sibling, (*x_nbr, c), (*y_nbr, c)))
                for s in sections]
        for cp in mine + sent:
            cp.start()
        for s in sections:
            copy(1, s, (*x_nbr, c), me).wait_recv()
            copy(2, s, (*y_nbr, c), me).wait_recv()
            sent += [copy(3, s, (*relay_from, c), (*relay_to, c)),
                     copy(4, s, (*x_nbr, c), sibling), copy(5, s, (*y_nbr, c), sibling)]
            for cp in sent[-3:]:
                cp.start()
        for s in sections:
            copy(3, s, (*diagonal, c), me).wait_recv()
            sent.append(copy(6, s, (*diagonal, c), sibling))
            sent[-1].start()
        for s in sections:
            copy(0, s, sibling, me).wait_recv()
            for k, chip in ((4, x_nbr), (5, y_nbr), (6, diagonal)):
                copy(k, s, (*chip, 1 - c), me).wait_recv()
        for cp in sent:
            cp.wait_send()
        for cp in mine:
            cp.wait()

    return pl.pallas_call(
        body, name=name,
        out_shape=tuple(jax.ShapeDtypeStruct((N_DEV * s.shape[0], s.shape[1]), s.dtype) for s in shards)
        + (jax.ShapeDtypeStruct((1, 1), F32),),
        in_specs=[ANY] * ns,
        out_specs=(ANY,) * ns + (pl.BlockSpec(memory_space=pltpu.SMEM),),
        scratch_shapes=[pltpu.SemaphoreType.DMA((7 * ns,)), pltpu.SemaphoreType.DMA((7 * ns,)),
                        pltpu.SemaphoreType.DMA((ns,))],
    )(*shards)


HBM = pl.BlockSpec(memory_space=pltpu.HBM)
SEM = pl.BlockSpec(memory_space=pltpu.SEMAPHORE)
EFFECT = pltpu.SideEffectType.DATAFLOW_SIDE_EFFECTING


def _in_hbm(a):
    return pltpu.with_memory_space_constraint(a, pltpu.HBM)


def _first_hop_copies(shards, x_refs, zones, send_sems, recv_sems):
    ns = len(shards)
    x, y, c = _position()
    targets = [(x, y, 1 - c)] + [(cx, cy, c) for cx, cy in _other_chips(x, y)]
    return [pltpu.make_async_remote_copy(
        src_ref=x_refs[s], dst_ref=_block_of(zones[s], 4 * x + 2 * y + c, shards[s].shape[0]),
        send_sem=send_sems.at[k * ns + s], recv_sem=recv_sems.at[k * ns + s],
        device_id=to, device_id_type=MESH_ID)
        for k, to in enumerate(targets) for s in range(ns)]


def _after_all(arrays, name):
    def body(*refs):
        refs[-1][...] = jnp.zeros_like(refs[-1])

    return pl.pallas_call(
        body, name=name,
        out_shape=jax.ShapeDtypeStruct((8, LANES), F32),
        in_specs=[pl.BlockSpec(memory_space=pl.ANY)] * len(arrays),
        out_specs=pl.BlockSpec(memory_space=pltpu.VMEM),
    )(*arrays)


def _own_blocks_placed(shards, after):
    ns = len(shards)
    x, y, c = _position()
    me = jnp.reshape(4 * x + 2 * y + c, (1,)).astype(jnp.int32)
    shards = [*shards[:-1], shards[-1] + after.astype(shards[-1].dtype)]

    def body(me_ref, *refs):
        for s in range(ns):
            refs[ns + s][...] = refs[s][...]

    return pl.pallas_call(
        body, name="own_blocks_placed",
        out_shape=tuple(jax.ShapeDtypeStruct((N_DEV * s.shape[0], s.shape[1]), s.dtype) for s in shards),
        grid_spec=pltpu.PrefetchScalarGridSpec(
            num_scalar_prefetch=1, grid=(1,),
            in_specs=[pl.BlockSpec(s.shape, lambda i, me: (0, 0)) for s in shards],
            out_specs=tuple(pl.BlockSpec(s.shape, lambda i, me: (me[0], 0)) for s in shards)),
        compiler_params=_params(dimension_semantics=("arbitrary",)),
    )(me, *shards)


def _gather_start(shards, after, name):
    ns = len(shards)
    zones = _own_blocks_placed(shards, after)

    def body(*refs):
        for cp in _first_hop_copies(shards, refs[:ns], refs[ns:2 * ns], refs[2 * ns], refs[2 * ns + 1]):
            cp.start()
        refs[-1][...] = jnp.zeros_like(refs[-1])

    out = pl.pallas_call(
        body, name=name,
        out_shape=(pltpu.SemaphoreType.DMA((4 * ns,)), pltpu.SemaphoreType.DMA((4 * ns,)),
                   *[pltpu.HBM(a.shape, a.dtype) for a in (*shards, *zones)],
                   jax.ShapeDtypeStruct((8, LANES), F32)),
        in_specs=[HBM] * (2 * ns),
        out_specs=(SEM, SEM, *[HBM] * (2 * ns), pl.BlockSpec(memory_space=pltpu.VMEM)),
        input_output_aliases={i: 2 + i for i in range(2 * ns)},
        compiler_params=pltpu.CompilerParams(has_side_effects=EFFECT),
    )(*[_in_hbm(a) for a in shards], *[_in_hbm(a) for a in zones])
    return out[0], out[1], out[2:2 + ns], out[2 + ns:2 + 2 * ns], out[-1]


def _gather_wait(send_sems, recv_sems, shards, zones, which, after, name):
    ns = len(shards)

    def body(*refs):
        copies = _first_hop_copies(shards, refs[:ns], refs[ns:2 * ns], refs[2 * ns], refs[2 * ns + 1])
        for i, cp in enumerate(copies):
            if i % ns in which:
                cp.wait_send()
                cp.wait_recv()

    out = pl.pallas_call(
        body, name=name,
        out_shape=tuple(pltpu.HBM(a.shape, a.dtype) for a in (*shards, *zones)),
        in_specs=[HBM] * (2 * ns) + [SEM, SEM, ANY],
        out_specs=(HBM,) * (2 * ns),
        input_output_aliases={i: i for i in range(2 * ns)},
        compiler_params=pltpu.CompilerParams(has_side_effects=EFFECT),
    )(*shards, *zones, send_sems, recv_sems, after)
    return out[:ns], out[ns:]


def _gather_pass_on(rows, zones, barrier_id, name):
    ns = len(zones)

    def body(*refs):
        in_refs, out_refs = refs[:ns], refs[ns:2 * ns]
        send_sems, recv_sems = refs[2 * ns:]
        x, y, c = _position()
        _pair_handshake(x, y, c)
        copies = [pltpu.make_async_remote_copy(
            src_ref=_block_of(in_refs[s], 4 * cx + 2 * cy + c, rows[s]),
            dst_ref=_block_of(out_refs[s], 4 * cx + 2 * cy + c, rows[s]),
            send_sem=send_sems.at[j * ns + s], recv_sem=recv_sems.at[j * ns + s],
            device_id=(x, y, 1 - c), device_id_type=MESH_ID)
            for j, (cx, cy) in enumerate(_other_chips(x, y)) for s in range(ns)]
        for cp in copies:
            cp.start()
        for cp in copies:
            cp.wait_recv()
        for cp in copies:
            cp.wait_send()

    return pl.pallas_call(
        body, name=name,
        out_shape=tuple(jax.ShapeDtypeStruct(z.shape, z.dtype) for z in zones),
        in_specs=[ANY] * ns, out_specs=(ANY,) * ns,
        input_output_aliases={i: i for i in range(ns)},
        scratch_shapes=[pltpu.SemaphoreType.DMA((3 * ns,)), pltpu.SemaphoreType.DMA((3 * ns,))],
        compiler_params=pltpu.CompilerParams(collective_id=barrier_id),
    )(*zones)


def _pair_copies(sections, g_refs, land, send_sems, recv_sems):
    ns = len(sections)
    x, y, c = _position()
    return [pltpu.make_async_remote_copy(
        src_ref=_block_of(g_refs[s], 2 * k + 1 - c, rows), dst_ref=land[s].at[k],
        send_sem=send_sems.at[k * ns + s], recv_sem=recv_sems.at[k * ns + s],
        device_id=(x, y, 1 - c), device_id_type=MESH_ID)
        for k in range(N_CHIPS) for s, (_, rows, _) in enumerate(sections)]


def _pair_exchange_start(sections, grads, barrier_id, name):
    ns = len(sections)

    def body(*refs):
        _pair_handshake(*_position())
        for cp in _pair_copies(sections, refs[:ns], refs[ns:2 * ns], refs[2 * ns], refs[2 * ns + 1]):
            cp.start()
        refs[-1][...] = jnp.zeros_like(refs[-1])

    zones = [lax.empty((N_CHIPS, rows, cols), BF16) for _, rows, cols in sections]
    n = N_CHIPS * ns
    out = pl.pallas_call(
        body, name=name,
        out_shape=(pltpu.SemaphoreType.DMA((n,)), pltpu.SemaphoreType.DMA((n,)),
                   *[pltpu.HBM(a.shape, a.dtype) for a in (*grads, *zones)],
                   jax.ShapeDtypeStruct((8, LANES), F32)),
        in_specs=[HBM] * (2 * ns),
        out_specs=(SEM, SEM, *[HBM] * (2 * ns), pl.BlockSpec(memory_space=pltpu.VMEM)),
        input_output_aliases={i: 2 + i for i in range(2 * ns)},
        compiler_params=pltpu.CompilerParams(has_side_effects=EFFECT, collective_id=barrier_id),
    )(*[_in_hbm(a) for a in grads], *[_in_hbm(a) for a in zones])
    return out[0], out[1], out[2:2 + ns], out[2 + ns:2 + 2 * ns], out[-1]


def _pair_exchange_wait(sections, send_sems, recv_sems, grads, zones, after, name):
    ns = len(sections)

    def body(*refs):
        for cp in _pair_copies(sections, refs[:ns], refs[ns:2 * ns], refs[2 * ns], refs[2 * ns + 1]):
            cp.wait_send()
            cp.wait_recv()

    out = pl.pallas_call(
        body, name=name,
        out_shape=tuple(pltpu.HBM(a.shape, a.dtype) for a in (*grads, *zones)),
        in_specs=[HBM] * (2 * ns) + [SEM, SEM, ANY],
        out_specs=(HBM,) * (2 * ns),
        input_output_aliases={i: i for i in range(2 * ns)},
        compiler_params=pltpu.CompilerParams(has_side_effects=EFFECT),
    )(*grads, *zones, send_sems, recv_sems, after)
    return out[:ns], out[ns:]


def _pair_add(sections, grads, got, core, name):
    ns = len(sections)

    def body(core_ref, *refs):
        g_refs, got_refs, p_refs = refs[:ns], refs[ns:2 * ns], refs[2 * ns:]
        for s in range(ns):
            p_refs[s][0] = (g_refs[s][...].astype(F32) + got_refs[s][0].astype(F32)).astype(BF16)

    slot = [pl.BlockSpec((1, rows, cols), lambda k, c: (k, 0, 0)) for _, rows, cols in sections]
    return pl.pallas_call(
        body, name=name,
        out_shape=tuple(jax.ShapeDtypeStruct((N_CHIPS, rows, cols), BF16) for _, rows, cols in sections),
        grid_spec=pltpu.PrefetchScalarGridSpec(
            num_scalar_prefetch=1, grid=(N_CHIPS,),
            in_specs=[pl.BlockSpec((rows, cols), lambda k, c: (2 * k + c[0], 0)) for _, rows, cols in sections]
            + slot,
            out_specs=tuple(slot)),
        compiler_params=_params(dimension_semantics=("parallel",)),
    )(core, *grads, *got)


def _chip_copies(sections, p_refs, land, send_sems, recv_sems):
    ns = len(sections)
    x, y, c = _position()
    return [pltpu.make_async_remote_copy(
        src_ref=p_refs[s].at[2 * cx + cy], dst_ref=land[s].at[j],
        send_sem=send_sems.at[j * ns + s], recv_sem=recv_sems.at[j * ns + s],
        device_id=(cx, cy, c), device_id_type=MESH_ID)
        for j, (cx, cy) in enumerate(_other_chips(x, y)) for s in range(ns)]


def _chip_exchange(sections, parts, name):
    ns = len(sections)

    def body(*refs):
        copies = _chip_copies(sections, refs[:ns], refs[ns:2 * ns], *refs[2 * ns:])
        for cp in copies:
            cp.start()
        for cp in copies:
            cp.wait_recv()
        for cp in copies:
            cp.wait_send()

    n = 3 * ns
    return pl.pallas_call(
        body, name=name,
        out_shape=tuple(jax.ShapeDtypeStruct((3, rows, cols), BF16) for _, rows, cols in sections),
        in_specs=[ANY] * ns, out_specs=(ANY,) * ns,
        scratch_shapes=[pltpu.SemaphoreType.DMA((n,)), pltpu.SemaphoreType.DMA((n,))],
    )(*parts)


def _chip_exchange_start(sections, parts, name):
    ns = len(sections)

    def body(*refs):
        p_refs, land = refs[:ns], refs[ns:2 * ns]
        send_sems, recv_sems = refs[2 * ns], refs[2 * ns + 1]
        token = refs[-1]
        for cp in _chip_copies(sections, p_refs, land, send_sems, recv_sems):
            cp.start()
        token[...] = jnp.zeros_like(token)

    zones = [lax.empty((3, rows, cols), BF16) for _, rows, cols in sections]
    out = pl.pallas_call(
        body, name=name,
        out_shape=(pltpu.SemaphoreType.DMA((3 * ns,)), pltpu.SemaphoreType.DMA((3 * ns,)),
                   *[pltpu.HBM(a.shape, a.dtype) for a in parts], *[pltpu.HBM(a.shape, a.dtype) for a in zones],
                   jax.ShapeDtypeStruct((8, LANES), F32)),
        in_specs=[HBM] * (2 * ns),
        out_specs=(SEM, SEM, *[HBM] * (2 * ns), pl.BlockSpec(memory_space=pltpu.VMEM)),
        input_output_aliases={i: 2 + i for i in range(2 * ns)},
        compiler_params=pltpu.CompilerParams(has_side_effects=EFFECT),
    )(*[_in_hbm(a) for a in parts], *[_in_hbm(a) for a in zones])
    return out[0], out[1], out[2:2 + ns], out[2 + ns:2 + 2 * ns], out[-1]


def _chip_exchange_wait(sections, send_sems, recv_sems, parts, zones, after, name):
    ns = len(sections)

    def body(*refs):
        p_refs, land = refs[:ns], refs[ns:2 * ns]
        for cp in _chip_copies(sections, p_refs, land, refs[2 * ns], refs[2 * ns + 1]):
            cp.wait_send()
            cp.wait_recv()

    out = pl.pallas_call(
        body, name=name,
        out_shape=tuple(pltpu.HBM(a.shape, a.dtype) for a in (*parts, *zones)),
        in_specs=[HBM] * (2 * ns) + [SEM, SEM, ANY],
        out_specs=(HBM,) * (2 * ns),
        input_output_aliases={i: i for i in range(2 * ns)},
        compiler_params=pltpu.CompilerParams(has_side_effects=EFFECT),
    )(*parts, *zones, send_sems, recv_sems, after)
    return out[:ns], out[ns:]


def _grad_finish(sections, parts, far, chip, name):
    ns = len(sections)

    def body(chip_ref, *refs):
        p_refs, b_refs, g_refs = refs[:ns], refs[ns:2 * ns], refs[2 * ns:]
        for s in range(ns):
            g = p_refs[s][0].astype(F32)
            for j in range(3):
                g = g + b_refs[s][j].astype(F32)
            g_refs[s][...] = g

    half = [(rows // 2, cols) for _, rows, cols in sections]
    return pl.pallas_call(
        body, name=name,
        out_shape=tuple(jax.ShapeDtypeStruct((rows, cols), F32) for _, rows, cols in sections),
        grid_spec=pltpu.PrefetchScalarGridSpec(
            num_scalar_prefetch=1, grid=(2,),
            in_specs=[pl.BlockSpec((1, r, c), lambda i, chip: (chip[0], i, 0)) for r, c in half]
            + [pl.BlockSpec((3, r, c), lambda i, chip: (0, i, 0)) for r, c in half],
            out_specs=tuple(pl.BlockSpec((r, c), lambda i, chip: (i, 0)) for r, c in half)),
        compiler_params=_params(dimension_semantics=("parallel",)),
    )(chip, *parts, *far)


def _sum_devices(parts, rows, name):
    cols = parts.shape[1]
    tr = rows // 2

    def body(*refs):
        s = refs[0][...].astype(F32)
        for d in range(1, N_DEV):
            s = s + refs[d][...].astype(F32)
        refs[N_DEV][...] = s

    return pl.pallas_call(
        body, name=name,
        out_shape=jax.ShapeDtypeStruct((rows, cols), F32),
        grid=(2,),
        in_specs=[pl.BlockSpec((tr, cols), lambda i, d=d: (2 * d + i, 0)) for d in range(N_DEV)],
        out_specs=pl.BlockSpec((tr, cols), lambda i: (i, 0)),
        compiler_params=_params(dimension_semantics=("parallel",)),
    )(*([parts] * N_DEV))


ADAMW_BLOCK = 8 * 1024
ADAMW_BLOCK_COLS = 512


def _adamw_step(w_ref, g_ref, m_ref, v_ref, d_ref, nm_ref, nv_ref, grad=None):
    c1 = 1.0 / (1.0 - ADAM_B1 ** ADAM_STEP)
    c2 = 1.0 / (1.0 - ADAM_B2 ** ADAM_STEP)

    def block(at):
        gv = g_ref[at] if grad is None else grad(at)
        nm = ADAM_B1 * m_ref[at] + (1.0 - ADAM_B1) * gv
        nv = ADAM_B2 * v_ref[at] + (1.0 - ADAM_B2) * (gv * gv)
        nm_ref[at] = nm
        nv_ref[at] = nv
        d_ref[at] = (-ADAM_LR) * ((nm * c1) / (jnp.sqrt(nv * c2) + ADAM_EPS) + ADAM_WD * w_ref[at])

    rows, cols = w_ref.shape
    bc = min(cols, ADAMW_BLOCK_COLS)
    br = ADAMW_BLOCK // bc
    if rows % br or cols % bc:
        block((slice(None), slice(None)))
        return

    def some_rows(i, carry):
        r = pl.ds(pl.multiple_of(i * br, br), br)
        for c in range(0, cols, bc):
            block((r, pl.ds(c, bc)))
        return carry

    lax.fori_loop(0, rows // br, some_rows, 0)


def _adamw_small(params, name):
    n = len(params)

    def body(*refs):
        for k in range(n):
            _adamw_step(*refs[4 * k:4 * k + 4], *refs[4 * n + 3 * k:4 * n + 3 * k + 3])

    out = pl.pallas_call(
        body, name=name,
        out_shape=tuple(jax.ShapeDtypeStruct(p[0].shape, F32) for p in params for _ in range(3)),
    )(*[a for p in params for a in p])
    return [out[3 * k:3 * k + 3] for k in range(n)]


ADAMW_TILE_BYTES = 1 << 20


def _adamw_rows(rows, cols):
    fits = [tr for tr in range(16, rows + 1, 16) if rows % tr == 0 and tr * cols * 4 <= ADAMW_TILE_BYTES]
    return max(fits, default=rows)


def _adamw(w, g, m, v, name):
    rows, cols = w.shape
    tr = _adamw_rows(rows, cols)

    def body(*refs):
        _adamw_step(*refs)

    spec = pl.BlockSpec((tr, cols), lambda i: (i, 0))
    shape = jax.ShapeDtypeStruct((rows, cols), F32)
    return pl.pallas_call(
        body, name=name,
        out_shape=(shape, shape, shape),
        grid=(rows // tr,),
        in_specs=[spec] * 4, out_specs=(spec,) * 3,
        compiler_params=_params(dimension_semantics=("parallel",)),
    )(w, g, m, v)


def _adamw_from_parts(w, m, v, parts, far, chip, name):
    rows, cols = w.shape
    tr = _adamw_rows(rows, cols)

    def body(chip_ref, w_ref, m_ref, v_ref, p_ref, b_ref, g_ref, d_ref, nm_ref, nv_ref):
        def grad(at):
            g = p_ref[(0, *at)].astype(F32)
            for j in range(3):
                g = g + b_ref[(j, *at)].astype(F32)
            g_ref[at] = g
            return g

        _adamw_step(w_ref, g_ref, m_ref, v_ref, d_ref, nm_ref, nv_ref, grad)

    spec = pl.BlockSpec((tr, cols), lambda i, c: (i, 0))
    shape = jax.ShapeDtypeStruct((rows, cols), F32)
    return pl.pallas_call(
        body, name=name,
        out_shape=(shape,) * 4,
        grid_spec=pltpu.PrefetchScalarGridSpec(
            num_scalar_prefetch=1, grid=(rows // tr,),
            in_specs=[spec, spec, spec,
                      pl.BlockSpec((1, tr, cols), lambda i, c: (c[0], i, 0)),
                      pl.BlockSpec((3, tr, cols), lambda i, c: (0, i, 0))],
            out_specs=(spec,) * 4),
        compiler_params=_params(dimension_semantics=("parallel",)),
    )(chip, w, m, v, parts, far)


NAMES = ("ln1_g", "w_in", "b_in", "rpb", "w_att_o", "conv_w", "conv_b", "w_rg_a", "b_rg_a", "w_rg_i",
         "b_rg_i", "lru_lambda", "w_rec_o", "w_out", "ln2_g", "w_ff1", "w_ff2", "lnf_g")
TRANSPOSED = {"w_in": "w_in_t", "w_att_o": "w_att_o_t", "w_ff1": "w_ff1_t"}
ROW_SHARDED = ("w_rec_o", "w_out", "w_ff2")
REPLICATED = (("ln1_g", (1, D)), ("b_in", (1, D_IN)), ("rpb", (N_HEADS * N_RPB_R, N_RPB_C)),
              ("conv_b", (1, D_REC)), ("w_rg_a", (2 * N_REC_BLOCKS * REC_BLOCK, REC_BLOCK)),
              ("w_rg_i", (2 * N_REC_BLOCKS * REC_BLOCK, REC_BLOCK)), ("ln2_g", (1, D)), ("lnf_g", (1, D)))
GATE_BLOCKS = ("w_rg_a", "w_rg_i")
SMALL_ROWS = 112


def _chan_bits(vectors):
    chan = jnp.concatenate(vectors, axis=0)
    bits = lax.bitcast_convert_type(chan, BF16).reshape(-1)
    return jnp.pad(bits, (0, CHAN_BLOCK_ROWS * D - bits.shape[0])).reshape(CHAN_BLOCK_ROWS, D)


def _chan_from_bits(gathered):
    bits = gathered.reshape(N_DEV, CHAN_BLOCK_ROWS * D)[:, :2 * N_CHAN_ROWS * LANES]
    chan = lax.bitcast_convert_type(bits.reshape(N_DEV, N_CHAN_ROWS, LANES, 2), F32)
    return chan.transpose(1, 0, 2).reshape(N_CHAN_ROWS, D)


def kernel(x, ln1_g, w_in, b_in, rpb, w_att_o, conv_w, conv_b, w_rg_a, b_rg_a, w_rg_i, b_rg_i, lru_lambda, w_rec_o, w_out, ln2_g, w_ff1, w_ff2, lnf_g, loss_target, m_ln1_g, m_w_in, m_b_in, m_rpb, m_w_att_o, m_conv_w, m_conv_b, m_w_rg_a, m_b_rg_a, m_w_rg_i, m_b_rg_i, m_lru_lambda, m_w_rec_o, m_w_out, m_ln2_g, m_w_ff1, m_w_ff2, m_lnf_g, v_ln1_g, v_w_in, v_b_in, v_rpb, v_w_att_o, v_conv_w, v_conv_b, v_w_rg_a, v_b_rg_a, v_w_rg_i, v_b_rg_i, v_lru_lambda, v_w_rec_o, v_w_out, v_ln2_g, v_w_ff1, v_w_ff2, v_lnf_g):
    w = dict(zip(NAMES, (ln1_g, w_in, b_in, rpb, w_att_o, conv_w, conv_b, w_rg_a, b_rg_a, w_rg_i,
                         b_rg_i, lru_lambda, w_rec_o, w_out, ln2_g, w_ff1, w_ff2, lnf_g)))
    m = dict(zip(NAMES, (m_ln1_g, m_w_in, m_b_in, m_rpb, m_w_att_o, m_conv_w, m_conv_b, m_w_rg_a,
                         m_b_rg_a, m_w_rg_i, m_b_rg_i, m_lru_lambda, m_w_rec_o, m_w_out, m_ln2_g,
                         m_w_ff1, m_w_ff2, m_lnf_g)))
    v = dict(zip(NAMES, (v_ln1_g, v_w_in, v_b_in, v_rpb, v_w_att_o, v_conv_w, v_conv_b, v_w_rg_a,
                         v_b_rg_a, v_w_rg_i, v_b_rg_i, v_lru_lambda, v_w_rec_o, v_w_out, v_ln2_g,
                         v_w_ff1, v_w_ff2, v_lnf_g)))
    xi, yi, ci = _position()

    shard = {t: w[n][0].T.astype(BF16) for n, t in TRANSPOSED.items()}
    shard.update({n: w[n][0].astype(BF16) for n in ROW_SHARDED})
    shard["chan"] = _chan_bits([w[n][0] for n, _ in CHAN])
    first, later = ("w_in_t", "chan"), ("w_rec_o", "w_out", "w_att_o_t", "w_ff1_t", "w_ff2")
    *gathered, done = _all_gather([shard[n] for n in first], "weight_all_gather")
    p = dict(zip(first, gathered))
    send_sems, recv_sems, sent, zones, token = _gather_start([shard[n] for n in later], done,
                                                             "weight_gather_start")

    travelling = {"shards": sent, "zones": zones}
    stages = (("w_rec_o", "w_out", "w_att_o_t"), ("w_ff1_t", "w_ff2"))

    def late_weights(after, stage):
        which = [later.index(n) for n in stages[stage]]
        travelling["shards"], travelling["zones"] = _gather_wait(
            send_sems, recv_sems, travelling["shards"], travelling["zones"], which, after,
            "weight_gather_wait_%d" % stage)
        return dict(zip(stages[stage], _gather_pass_on(
            [shard[n].shape[0] for n in stages[stage]], [travelling["zones"][i] for i in which],
            PASS_ON_IDS[stage], "weight_gather_pass_on_%d" % stage)))

    chan = _chan_from_bits(p.pop("chan"))
    r0 = 0
    for n, rows in CHAN:
        p[n] = chan[r0:r0 + rows]
        r0 += rows
    p.update(ln1_g=w["ln1_g"], b_in=w["b_in"] + token[0, 0], rpb=w["rpb"][0], conv_b=w["conv_b"],
             w_rg_a=w["w_rg_a"][0], w_rg_i=w["w_rg_i"][0], ln2_g=w["ln2_g"],
             lnf_g=w["lnf_g"].reshape(1, D))

    core = jnp.reshape(ci, (1,)).astype(jnp.int32)
    chip = jnp.reshape(2 * xi + yi, (1,)).astype(jnp.int32)
    first_sections = tuple(s for s in SECTIONS if s[0] in ("w_ff1_t", "w_ff2"))
    late_sections = SECTIONS[:1]
    early_sections = tuple(s for s in SECTIONS[1:] if s not in first_sections)
    in_flight = {}

    def pair_sum_and_send(group, sections, after):
        send_sems, recv_sems, sect, zones, _ = in_flight["pair_" + group]
        sect, got = _pair_exchange_wait(sections, send_sems, recv_sems, sect, zones, after,
                                        "grad_pair_exchange_wait_" + group)
        parts = _pair_add(sections, sect, got, core, "grad_pair_add_" + group)
        in_flight[group] = _chip_exchange_start(sections, parts, "grad_chip_exchange_start_" + group)
        return in_flight[group][-1]

    def pair_exchange_at_once(group, sections, grads, barrier_id):
        in_flight["pair_" + group] = _pair_exchange_start(
            sections, [grads[n] for n, _, _ in sections], barrier_id, "grad_pair_exchange_start_" + group)
        return pair_sum_and_send(group, sections, in_flight["pair_" + group][-1])

    def reduce_first(grads, after):
        if grads is None:
            return pair_sum_and_send("first", first_sections, after)
        in_flight["pair_first"] = _pair_exchange_start(
            first_sections, [grads[n] for n, _, _ in first_sections], PAIR_FIRST_ID,
            "grad_pair_exchange_start_first")
        return in_flight["pair_first"][-1]

    def reduce_early(grads):
        chan_g = jnp.concatenate([grads[n] for n, _ in CHAN], axis=0)
        chan_g = chan_g.reshape(N_CHAN_ROWS, N_DEV, LANES).transpose(1, 0, 2).astype(BF16)
        chan_g = jnp.pad(chan_g.reshape(N_DEV, -1), ((0, 0), (0, CHAN_BLOCK_ROWS * D - N_CHAN_ROWS * LANES)))
        grads["chan"] = chan_g.reshape(N_DEV * CHAN_BLOCK_ROWS, D)
        grads["gates"] = jnp.concatenate([grads[n].reshape(-1, D) for n in GATE_BLOCKS], axis=0).astype(BF16)
        return pair_exchange_at_once("early", early_sections, grads, PAIR_EARLY_ID)[0, 0]

    loss_part, grad_x, grads = _local_step(x[0], loss_target[0], p, late_weights, reduce_first, reduce_early)
    in_flight["pair_late"] = _pair_exchange_start(
        late_sections, [grads[n] for n, _, _ in late_sections], PAIR_LATE_ID, "grad_pair_exchange_start_late")

    arrived = {}

    def finish(group, sections, after, name):
        send_sems, recv_sems, parts, zones, _ = in_flight[group]
        parts, far = _chip_exchange_wait(sections, send_sems, recv_sems, parts, zones, after,
                                         "grad_chip_exchange_wait_" + name)
        keep = [i for i, s in enumerate(sections) if s[0] not in (*ROW_SHARDED, "w_in_t")]
        arrived.update({s[0]: (parts[i], far[i]) for i, s in enumerate(sections) if i not in keep})
        if not keep:
            return {}
        return dict(zip((sections[i][0] for i in keep),
                        _grad_finish([sections[i] for i in keep], [parts[i] for i in keep],
                                     [far[i] for i in keep], chip, "grad_finish_" + name)))

    summed = finish("first", first_sections, in_flight["pair_late"][-1], "first")
    summed.update(finish("early", early_sections, summed["w_ff1_t"], "early"))
    started_late = pair_sum_and_send("late", late_sections, summed["gates"])

    flat = jnp.concatenate([grads[n].reshape(-1) for n, _ in REPLICATED if n not in GATE_BLOCKS]
                           + [loss_part.reshape(-1) + started_late[0, 0]])
    n_small = flat.shape[0]
    flat = jnp.pad(flat, (0, SMALL_ROWS * LANES - n_small)).reshape(SMALL_ROWS, LANES)
    small_parts, gate_sum, _ = _all_gather([flat, summed["gates"]], "small_grad_all_gather")
    small = _sum_devices(small_parts, SMALL_ROWS, "small_grad_sum").reshape(-1)
    loss = small[n_small - 1]

    g, delta, new_m, new_v = {}, {}, {}, {}

    def update(n, g2, shape2):
        d2, m2, v2 = _adamw(w[n].reshape(shape2), g2, m[n].reshape(shape2), v[n].reshape(shape2),
                            "adamw_" + n)
        g[n], delta[n], new_m[n], new_v[n] = (a.reshape(w[n].shape) for a in (g2, d2, m2, v2))

    small_params = []
    o = 0
    for n, shape2 in REPLICATED:
        if n in GATE_BLOCKS:
            k, rows = GATE_BLOCKS.index(n), gate_sum.shape[0] // len(GATE_BLOCKS)
            update(n, gate_sum[k * rows:(k + 1) * rows].reshape(shape2), shape2)
        else:
            size = shape2[0] * shape2[1]
            small_params.append((n, small[o:o + size].reshape(shape2), shape2))
            o += size
    chan_back = summed["chan"].reshape(-1)[:N_CHAN_ROWS * LANES].reshape(N_CHAN_ROWS, LANES)
    r0 = 0
    for n, rows in CHAN:
        small_params.append((n, chan_back[r0:r0 + rows], (rows, LANES)))
        r0 += rows
    results = _adamw_small([(w[n].reshape(s2), g2, m[n].reshape(s2), v[n].reshape(s2))
                            for n, g2, s2 in small_params], "adamw_vectors")
    for (n, g2, _), (d2, m2, v2) in zip(small_params, results):
        g[n], delta[n], new_m[n], new_v[n] = (a.reshape(w[n].shape) for a in (g2, d2, m2, v2))

    for n in ROW_SHARDED:
        results = _adamw_from_parts(w[n][0], m[n][0], v[n][0], *arrived[n], chip, "adamw_" + n)
        g[n], delta[n], new_m[n], new_v[n] = (a[None] for a in results)
    for n, t in TRANSPOSED.items():
        if t in summed:
            update(n, summed[t].T, summed[t].shape[::-1])
    finish("late", late_sections, _after_all(list(delta.values()), "updates_done"), "late")
    results = _adamw_from_parts(w["w_in"][0].T, m["w_in"][0].T, v["w_in"][0].T, *arrived["w_in_t"], chip,
                                "adamw_w_in")
    g["w_in"], delta["w_in"], new_m["w_in"], new_v["w_in"] = (a.T[None] for a in results)

    return (loss, grad_x[None], *[g[n] for n in NAMES], *[delta[n] for n in NAMES],
            *[new_m[n] for n in NAMES], *[new_v[n] for n in NAMES])
```

```python
import math

import numpy as np
import jax
import jax.numpy as jnp
from jax import lax
from jax.experimental import pallas as pl
from jax.experimental.pallas import tpu as pltpu

F32 = jnp.float32
BF16 = jnp.bfloat16

T = 2048
D = 1024
D_ATT = 512
D_REC = 1024
D_FF = 4096
D_IN = 5632
N_HEADS = 8
DH = 64
GRID_W = 64
ROWS = T // GRID_W
WIN_H = 8
WIN_W = 16
KWIN = WIN_H * GRID_W
N_RPB_R = 2 * WIN_H - 1
N_RPB_C = 2 * WIN_W - 1
N_REC_BLOCKS = 16
REC_BLOCK = 64
CG = 128
N_CG = D_REC // CG
LRU_C = 8.0
EPS = 1e-6
N_DEV = 8
N_CHIPS = 4
LANES = 128

ADAM_LR = 0.001
ADAM_B1 = 0.9
ADAM_B2 = 0.999
ADAM_EPS = 1e-08
ADAM_WD = 0.01
ADAM_STEP = 10

MESH_AXES = ("x", "y", "c")
VMEM_LIMIT = 56 * 1024 * 1024

TILE = 512
DZ_ARRAYS = ((0, 3, 1), (3, 4, 2), (7, 4, 2))
N_DZ_TILES = D_IN // TILE


def _params(**kw):
    return pltpu.CompilerParams(vmem_limit_bytes=VMEM_LIMIT, **kw)


HG = 4
HQ = HG * GRID_W
HC = HG * DH


def _att_tables():
    rq = np.arange(GRID_W)
    kc = np.arange(KWIN) % GRID_W
    win_start = np.clip(rq - WIN_W // 2, 0, GRID_W - WIN_W)
    valid = (kc[None, :] >= win_start[:, None]) & (kc[None, :] < win_start[:, None] + WIN_W)
    same_head = (np.arange(HQ)[:, None] // GRID_W) == (np.arange(HC)[None, :] // DH)
    return valid.astype(np.float32), same_head.astype(np.float32)


def _pair_mask():
    half = np.arange(2 * DH) // DH
    return (half[:, None] == half[None, :]).astype(np.float32)


def _dup_table():
    return np.concatenate([np.eye(REC_BLOCK, dtype=np.float32)] * 2, axis=1)


def _sigmoid(x):
    return 0.5 * jnp.tanh(0.5 * x) + 0.5


def _softplus(x):
    return jnp.maximum(x, 0.0) + jnp.log(1.0 + jnp.exp(-jnp.abs(x)))


def _one_minus_square(log_a, a):
    x = 2.0 * log_a
    series = -x * (1.0 + x * (0.5 + x * (1.0 / 6.0)))
    return jnp.where(x > -0.02, series, 1.0 - a * a)


_GELU_C = math.sqrt(2.0 / math.pi)


def _gelu_and_grad(x):
    x2 = x * x
    inner = _GELU_C * (x + 0.044715 * x * x2)
    t = jnp.tanh(inner)
    g = 0.5 * x * (1.0 + t)
    dg = 0.5 * (1.0 + t) + 0.5 * x * (1.0 - t * t) * _GELU_C * (1.0 + 3.0 * 0.044715 * x2)
    return g, dg


def _dot(a, b):
    return jnp.dot(a, b, preferred_element_type=F32)


def _dot_nt(a, b):
    return lax.dot_general(a, b, (((1,), (1,)), ((), ())), preferred_element_type=F32)


def _dot_tn(a, b):
    return lax.dot_general(a, b, (((0,), (0,)), ((), ())), preferred_element_type=F32)


def _dot_exact(a, b):
    return jnp.dot(a, b, precision=lax.Precision.HIGHEST, preferred_element_type=F32)


def _shift_rows(x, s):
    n = x.shape[0]
    rows = lax.broadcasted_iota(jnp.int32, x.shape, 0)
    y = pltpu.roll(x, s % n, 0)
    if s > 0:
        return jnp.where(rows >= s, y, 0.0)
    return jnp.where(rows < n + s, y, 0.0)


def _rms_bwd(dh, xh, r, g):
    dxh = dh * g
    return r * (dxh - xh * jnp.mean(dxh * xh, axis=-1, keepdims=True))


def _matmul(a, b, mode, out_dtype, name, tm=512, tn=1024, tk=2048):
    if mode == "nn":
        (m, k), (k2, n) = a.shape, b.shape
    elif mode == "nt":
        (m, k), (n, k2) = a.shape, b.shape
    else:
        (k, m), (k2, n) = a.shape, b.shape
    assert k == k2
    tm, tn, tk = min(tm, m), min(tn, n), min(tk, k)
    assert m % tm == 0 and n % tn == 0 and k % tk == 0
    nk = k // tk
    dot = {"nn": _dot, "nt": _dot_nt, "tn": _dot_tn}[mode]

    def body(a_ref, b_ref, o_ref, acc):
        kk = pl.program_id(2)
        part = dot(a_ref[...].astype(BF16), b_ref[...].astype(BF16))
        if nk == 1:
            o_ref[...] = part.astype(out_dtype)
            return

        @pl.when(kk == 0)
        def _():
            acc[...] = part

        @pl.when(kk > 0)
        def _():
            acc[...] += part

        @pl.when(kk == nk - 1)
        def _():
            o_ref[...] = acc[...].astype(out_dtype)

    if mode == "tn":
        a_spec = pl.BlockSpec((tk, tm), lambda i, j, kk: (kk, i))
    else:
        a_spec = pl.BlockSpec((tm, tk), lambda i, j, kk: (i, kk))
    if mode == "nt":
        b_spec = pl.BlockSpec((tn, tk), lambda i, j, kk: (j, kk))
    else:
        b_spec = pl.BlockSpec((tk, tn), lambda i, j, kk: (kk, j))
    return pl.pallas_call(
        body, name=name,
        out_shape=jax.ShapeDtypeStruct((m, n), out_dtype),
        grid=(m // tm, n // tn, nk),
        in_specs=[a_spec, b_spec],
        out_specs=pl.BlockSpec((tm, tn), lambda i, j, kk: (i, j)),
        scratch_shapes=[pltpu.VMEM((tm, tn) if nk > 1 else (8, LANES), F32)],
        compiler_params=_params(dimension_semantics=("parallel", "parallel", "arbitrary")),
    )(a, b)


def _in_proj(x, g1, w_in_t, b_in):
    tm = 512

    def body(x_ref, g_ref, w_hbm, b_ref, qkv_ref, uy_ref, gg_ref, h_ref, w):
        @pl.when(pl.program_id(0) == 0)
        def _():
            pltpu.sync_copy(w_hbm, w)

        xv = x_ref[...]
        r = lax.rsqrt(jnp.mean(xv * xv, axis=-1, keepdims=True) + EPS)
        h = ((xv * r) * g_ref[...]).astype(BF16)
        h_ref[...] = h
        row0 = 0
        for ref in (qkv_ref, uy_ref, gg_ref):
            for c0 in range(0, ref.shape[1], TILE):
                z = _dot_nt(h, w[row0:row0 + TILE, :]) + b_ref[:, row0:row0 + TILE]
                ref[:, c0:c0 + TILE] = z.astype(ref.dtype)
                row0 += TILE

    tok = lambda width: pl.BlockSpec((tm, width), lambda i: (i, 0))
    return pl.pallas_call(
        body, name="in_proj",
        out_shape=(jax.ShapeDtypeStruct((T, 3 * D_ATT), BF16),
                   jax.ShapeDtypeStruct((T, 2 * D_REC), F32),
                   jax.ShapeDtypeStruct((T, 2 * D), F32),
                   jax.ShapeDtypeStruct((T, D), BF16)),
        grid=(T // tm,),
        in_specs=[tok(D), pl.BlockSpec((1, D), lambda i: (0, 0)), pl.BlockSpec(memory_space=pl.ANY),
                  pl.BlockSpec((1, D_IN), lambda i: (0, 0))],
        out_specs=(tok(3 * D_ATT), tok(2 * D_REC), tok(2 * D), tok(D)),
        scratch_shapes=[pltpu.VMEM((D_IN, D), BF16)],
        compiler_params=_params(dimension_semantics=("arbitrary",)),
    )(x, g1, w_in_t, b_in)


def _dz_specs(rows, tile_of, row_of):
    def spec(off, n, per_plane):
        def index(*ids):
            t = jnp.clip(tile_of(*ids) - off, 0, n - 1)
            return (t // per_plane, row_of(*ids), t % per_plane)
        return pl.BlockSpec((1, rows, TILE), index)
    return [spec(off, n, per) for off, n, per in DZ_ARRAYS]


def _dh_norm1_bwd(dz, w_in_t, x, g1, dx1, after):
    tm = 512

    def body(dqkv_ref, duy_ref, dgg_ref, w_hbm, x_ref, g_ref, dx1_ref, after_ref, gx_ref, dg_ref, w):
        @pl.when(pl.program_id(0) == 0)
        def _():
            pltpu.sync_copy(w_hbm, w)
            dg_ref[...] = jnp.zeros_like(dg_ref)

        dh, row0 = None, 0
        for ref in (dqkv_ref, duy_ref, dgg_ref):
            for plane in range(ref.shape[0]):
                cols = ref.shape[2]
                part = _dot(ref[plane], w[row0:row0 + cols, :])
                dh = part if dh is None else dh + part
                row0 += cols
        xv = x_ref[...]
        r = lax.rsqrt(jnp.mean(xv * xv, axis=-1, keepdims=True) + EPS)
        xh = xv * r
        dg_ref[...] += jnp.sum(dh * xh, axis=0, keepdims=True)
        gx_ref[...] = dx1_ref[...] + _rms_bwd(dh, xh, r, g_ref[...])

    tok = pl.BlockSpec((tm, D), lambda i: (i, 0))
    vec = pl.BlockSpec((1, D), lambda i: (0, 0))
    planes = lambda a: pl.BlockSpec((a.shape[0], tm, a.shape[2]), lambda i: (0, i, 0))
    return pl.pallas_call(
        body, name="dh_norm1_bwd",
        out_shape=(jax.ShapeDtypeStruct((T, D), F32), jax.ShapeDtypeStruct((1, D), F32)),
        grid=(T // tm,),
        in_specs=[planes(a) for a in dz] + [pl.BlockSpec(memory_space=pl.ANY), tok, vec, tok,
                                            pl.BlockSpec(memory_space=pl.ANY)],
        out_specs=(tok, vec),
        scratch_shapes=[pltpu.VMEM((D_IN, D), BF16)],
        compiler_params=_params(dimension_semantics=("arbitrary",)),
    )(*dz, w_in_t, x, g1, dx1, after)


def _grad_w_in(dz, h):
    def body(*refs):
        seg_refs = refs[:3]
        h_ref, gw_ref, gb_ref = refs[3:]
        j = pl.program_id(0)

        for s, (off, n, _) in enumerate(DZ_ARRAYS):
            @pl.when((j >= off) & (j < off + n))
            def _(s=s):
                a = seg_refs[s][0]
                gw_ref[...] = _dot_tn(a, h_ref[...]).astype(BF16)
                gb_ref[...] = jnp.sum(a.astype(F32), axis=0, keepdims=True)

    return pl.pallas_call(
        body, name="grad_w_in",
        out_shape=(jax.ShapeDtypeStruct((D_IN, D), BF16), jax.ShapeDtypeStruct((1, D_IN), F32)),
        grid=(N_DZ_TILES,),
        in_specs=_dz_specs(T, lambda j: j, lambda j: 0) + [pl.BlockSpec((T, D), lambda j: (0, 0))],
        out_specs=(pl.BlockSpec((TILE, D), lambda j: (j, 0)), pl.BlockSpec((1, TILE), lambda j: (0, j))),
        compiler_params=_params(dimension_semantics=("parallel",)),
    )(*dz, h)


def _rpb_rows(rpb):
    padded = jnp.pad(rpb, ((0, 0), (0, 0), (0, GRID_W - N_RPB_C)))
    rows = [padded[:, WIN_H - 1 - oi: 2 * WIN_H - 1 - oi].reshape(N_HEADS // HG, HG, KWIN)
            for oi in range(WIN_H)]
    return jnp.stack(rows, axis=0)


SKEW = KWIN - (WIN_W - 1)


MASKED = -1e30


def _bias_tiles(rows_ref, valid, bias_s):
    for oi in range(WIN_H):
        for hh in range(HG):
            row = jnp.broadcast_to(rows_ref[oi, 0, hh:hh + 1, :], (GRID_W, KWIN))
            tile = pltpu.roll(row, SKEW, 1, stride=1, stride_axis=0)
            bias_s[oi, hh * GRID_W:(hh + 1) * GRID_W, :] = jnp.where(valid, tile, MASKED)


def _bias_tile_grads(gb_s, flip, out_ref):
    for oi in range(WIN_H):
        for hh in range(HG):
            g = _dot_exact(flip, gb_s[oi, hh * GRID_W:(hh + 1) * GRID_W, :])
            back = pltpu.roll(g, KWIN - (GRID_W - WIN_W), 1, stride=1, stride_axis=0)
            out_ref[0, oi, hh:hh + 1, :] = jnp.sum(back, axis=0, keepdims=True)


def _rpb_fold(row_grads):
    g = row_grads.transpose(1, 0, 2, 3).reshape(WIN_H, N_HEADS, WIN_H, GRID_W)
    g = g.transpose(0, 2, 1, 3)

    def body(g_ref, o_ref):
        for dr in range(N_RPB_R):
            terms = [g_ref[oi, i] for oi in range(WIN_H) for i in range(WIN_H) if i - oi + WIN_H - 1 == dr]
            acc = terms[0]
            for term in terms[1:]:
                acc = acc + term
            o_ref[dr] = acc

    out = pl.pallas_call(
        body, name="rpb_fold",
        out_shape=jax.ShapeDtypeStruct((N_RPB_R, N_HEADS, GRID_W), F32),
    )(g)
    return out.transpose(1, 0, 2)[:, :, :N_RPB_C]


ATT_GROUPS = N_HEADS // HG
ATT_UNROLL = 8


def _stacked(rows64, same_head):
    return jnp.where(same_head, jnp.concatenate([rows64] * HG, axis=0), jnp.zeros((), BF16))


def _own_heads(stacked):
    head = lax.broadcasted_iota(jnp.int32, (GRID_W, HC), 1) // DH
    out = stacked[:GRID_W]
    for h in range(1, HG):
        out = jnp.where(head == h, stacked[h * GRID_W:(h + 1) * GRID_W], out)
    return out


def _att_scores(q_ref, k_ref, bias_ref, same_head, r):
    rs = jnp.clip(r - WIN_H // 2, 0, ROWS - WIN_H)
    oi = r - rs
    q0 = pl.multiple_of(r * GRID_W, GRID_W)
    k0 = pl.multiple_of(rs * GRID_W, GRID_W)
    q2 = _stacked(q_ref[pl.ds(q0, GRID_W), :] * (DH ** -0.5), same_head)
    kw = k_ref[pl.ds(k0, KWIN), :]
    s = _dot_nt(q2, kw) + bias_ref[oi]
    e = jnp.exp(s - jnp.max(s, axis=-1, keepdims=True))
    return e, 1.0 / jnp.sum(e, axis=-1, keepdims=True), q2, kw, q0, k0, oi


def _att_specs():
    col = lambda off: pl.BlockSpec((T, HC), lambda g: (0, g + off * ATT_GROUPS))
    tables = [pl.BlockSpec((WIN_H, 1, HG, KWIN), lambda g: (0, g, 0, 0)),
              pl.BlockSpec((GRID_W, KWIN), lambda g: (0, 0)),
              pl.BlockSpec((HQ, HC), lambda g: (0, 0))]
    return col, tables, pltpu.VMEM((WIN_H, HQ, KWIN), F32)


def _att_fwd(qkv, bias_rows):
    valid_np, same_head_np = _att_tables()

    def body(q_ref, k_ref, v_ref, rows_ref, valid_ref, head_ref, o_ref, bias_s):
        same_head = head_ref[...] > 0.5
        _bias_tiles(rows_ref, valid_ref[...] > 0.5, bias_s)

        def row(r, carry):
            e, rl, _, _, q0, k0, _ = _att_scores(q_ref, k_ref, bias_s, same_head, r)
            o2 = _dot((e * rl).astype(BF16), v_ref[pl.ds(k0, KWIN), :])
            o_ref[pl.ds(q0, GRID_W), :] = _own_heads(o2).astype(BF16)
            return carry

        lax.fori_loop(0, ROWS, row, 0, unroll=ATT_UNROLL)

    col, tables, tiles = _att_specs()
    return pl.pallas_call(
        body, name="att_fwd",
        out_shape=jax.ShapeDtypeStruct((T, D_ATT), BF16),
        grid=(ATT_GROUPS,),
        in_specs=[col(0), col(1), col(2)] + tables,
        out_specs=col(0),
        scratch_shapes=[tiles],
        compiler_params=_params(dimension_semantics=("parallel",)),
    )(qkv, qkv, qkv, bias_rows, jnp.asarray(valid_np), jnp.asarray(same_head_np))


def _att_bwd(qkv, bias_rows, datt, after):
    valid_np, same_head_np = _att_tables()

    def body(q_ref, k_ref, v_ref, do_ref, rows_ref, valid_ref, head_ref, flip_ref,
             dqkv_ref, grows_ref, dk_acc, dv_acc, bias_s, gb_s):
        same_head = head_ref[...] > 0.5
        dk_acc[...] = jnp.zeros_like(dk_acc)
        dv_acc[...] = jnp.zeros_like(dv_acc)
        gb_s[...] = jnp.zeros_like(gb_s)
        _bias_tiles(rows_ref, valid_ref[...] > 0.5, bias_s)

        def row(r, carry):
            e, rl, q2, kw, q0, k0, oi = _att_scores(q_ref, k_ref, bias_s, same_head, r)
            do2 = _stacked(do_ref[pl.ds(q0, GRID_W), :], same_head)
            vw = v_ref[pl.ds(k0, KWIN), :]
            p = e * rl
            dp = _dot_nt(do2, vw)
            ds = p * (dp - jnp.sum(dp * p, axis=-1, keepdims=True))
            p16 = p.astype(BF16)
            ds16 = ds.astype(BF16)
            dv_acc[pl.ds(k0, KWIN), :] += _dot_tn(p16, do2)
            dk_acc[pl.ds(k0, KWIN), :] += _dot_tn(ds16, q2)
            dq2 = _dot(ds16, kw) * (DH ** -0.5)
            dqkv_ref[0, pl.ds(q0, GRID_W), :] = _own_heads(dq2).astype(BF16)
            gb_s[oi] += ds
            return carry

        lax.fori_loop(0, ROWS, row, 0, unroll=ATT_UNROLL)
        dqkv_ref[1] = dk_acc[...].astype(BF16)
        dqkv_ref[2] = dv_acc[...].astype(BF16)
        _bias_tile_grads(gb_s, flip_ref[...], grows_ref)

    col, tables, tiles = _att_specs()
    return pl.pallas_call(
        body, name="att_bwd",
        out_shape=(jax.ShapeDtypeStruct((3, T, D_ATT), BF16),
                   jax.ShapeDtypeStruct((ATT_GROUPS, WIN_H, HG, KWIN), F32)),
        grid=(ATT_GROUPS,),
        in_specs=[col(0), col(1), col(2), col(0)] + tables + [pl.BlockSpec((GRID_W, GRID_W), lambda g: (0, 0))],
        out_specs=(pl.BlockSpec((3, T, HC), lambda g: (0, 0, g)),
                   pl.BlockSpec((1, WIN_H, HG, KWIN), lambda g: (g, 0, 0, 0))),
        scratch_shapes=[pltpu.VMEM((T, HC), F32), pltpu.VMEM((T, HC), F32), tiles, tiles],
        compiler_params=_params(dimension_semantics=("parallel",)),
    )(qkv, qkv, qkv, datt, bias_rows, jnp.asarray(valid_np) + after, jnp.asarray(same_head_np),
      jnp.asarray(np.eye(GRID_W, dtype=np.float32)[::-1].copy()))


def _conv_taps(up):
    return (_shift_rows(up, 2), _shift_rows(up, 1), up, _shift_rows(up, -1))


def _pair_block_diag(w_pair, dup, same_half):
    return jnp.where(same_half, _dot(w_pair.astype(BF16), dup), 0.0).astype(BF16)


def _gates(u, u16, wa, ba, wi, bi, lam):
    r = _sigmoid(_dot(u16, wa) + ba)
    ig = _sigmoid(_dot(u16, wi) + bi)
    sp = _softplus(-lam)
    log_a = (-LRU_C) * r * sp
    a = jnp.exp(log_a)
    mult2 = jnp.maximum(_one_minus_square(log_a, a), 0.0)
    return r, ig, sp, a, jnp.sqrt(mult2), mult2


SCAN_BLOCKS = 8


def _scans(jobs):
    c = jobs[0][0].shape[1]
    nblk = T // 8
    rows = lax.broadcasted_iota(jnp.int32, (8, c), 0)

    def block(a, b, reverse):
        for s in (1, 2, 4):
            if reverse:
                keep = rows < 8 - s
                a_s = jnp.where(keep, pltpu.roll(a, 8 - s, 0), 1.0)
                b_s = jnp.where(keep, pltpu.roll(b, 8 - s, 0), 0.0)
            else:
                keep = rows >= s
                a_s = jnp.where(keep, pltpu.roll(a, s, 0), 1.0)
                b_s = jnp.where(keep, pltpu.roll(b, s, 0), 0.0)
            b = a * b_s + b
            a = a * a_s
        return a, b

    def step(i, carry):
        out = []
        for (a_ref, b_ref, h_ref, reverse), h_prev in zip(jobs, carry):
            for u in range(SCAN_BLOCKS):
                blk = i * SCAN_BLOCKS + u
                if reverse:
                    blk = nblk - 1 - blk
                t0 = pl.multiple_of(blk * 8, 8)
                a, b = block(a_ref[pl.ds(t0, 8), :], b_ref[pl.ds(t0, 8), :], reverse)
                h = a * h_prev + b
                h_ref[pl.ds(t0, 8), :] = h
                h_prev = jnp.broadcast_to(h[0:1] if reverse else h[7:8], (8, c))
            out.append(h_prev)
        return tuple(out)

    lax.fori_loop(0, nblk // SCAN_BLOCKS, step, tuple(jnp.zeros((8, c), F32) for _ in jobs))


def _rec_specs():
    tok = lambda off: pl.BlockSpec((T, CG), lambda g: (0, g + off))
    per_ch = lambda rows: pl.BlockSpec((rows, CG), lambda g: (0, g))
    wspec = pl.BlockSpec((2, 1, CG, REC_BLOCK), lambda g: (0, g, 0, 0))
    const = lambda shape: pl.BlockSpec(shape, lambda g: (0, 0))
    return tok, per_ch, wspec, const


def _rec_fwd(uy, conv_w, conv_b, w_a, b_a, w_i, b_i, lam):
    tok, per_ch, wspec, const = _rec_specs()

    def body(up_ref, yb_ref, cw_ref, cb_ref, wa_ref, ba_ref, wi_ref, bi_ref, lam_ref, dup_ref, half_ref,
             hf_ref, hb_ref, yrec_ref, am_ref, bx_f, bx_b):
        dup = dup_ref[...]
        same_half = half_ref[...] > 0.5
        taps = _conv_taps(up_ref[...])
        u = cb_ref[...]
        for j in range(4):
            u = u + taps[j] * cw_ref[j:j + 1, :]
        u16 = u.astype(BF16)
        for d, bx_s in enumerate((bx_f, bx_b)):
            wa = _pair_block_diag(wa_ref[d, 0], dup, same_half)
            wi = _pair_block_diag(wi_ref[d, 0], dup, same_half)
            _, ig, _, a, mult, _ = _gates(u, u16, wa, ba_ref[d:d + 1, :], wi, bi_ref[d:d + 1, :],
                                       lam_ref[d:d + 1, :])
            am_ref[2 * d] = a
            am_ref[2 * d + 1] = mult
            bx_s[...] = mult * (ig * u)
        _scans([(am_ref.at[0], bx_f, hf_ref, False), (am_ref.at[2], bx_b, hb_ref, True)])
        gelu, _ = _gelu_and_grad(yb_ref[...])
        yrec_ref[...] = ((hf_ref[...] + hb_ref[...]) * gelu).astype(BF16)

    return pl.pallas_call(
        body, name="rec_fwd",
        out_shape=(jax.ShapeDtypeStruct((T, D_REC), F32), jax.ShapeDtypeStruct((T, D_REC), F32),
                   jax.ShapeDtypeStruct((T, D_REC), BF16), jax.ShapeDtypeStruct((4, T, D_REC), F32)),
        grid=(N_CG,),
        in_specs=[tok(0), tok(N_CG), per_ch(4), per_ch(1), wspec, per_ch(2), wspec, per_ch(2), per_ch(2),
                  const((REC_BLOCK, CG)), const((CG, CG))],
        out_specs=(tok(0), tok(0), tok(0), pl.BlockSpec((4, T, CG), lambda g: (0, 0, g))),
        scratch_shapes=[pltpu.VMEM((T, CG), F32)] * 2,
        compiler_params=_params(dimension_semantics=("parallel",)),
    )(uy, uy, conv_w, conv_b, w_a, b_a, w_i, b_i, lam,
      jnp.asarray(_dup_table(), BF16), jnp.asarray(_pair_mask()))


def _rec_bwd(uy, hf, hb, am, dyrec, conv_w, conv_b, w_a, b_a, w_i, b_i, lam):
    tok, per_ch, wspec, const = _rec_specs()

    def body(up_ref, yb_ref, hf_ref, hb_ref, am_ref, dy_ref, cw_ref, cb_ref, wa_ref, ba_ref, wi_ref, bi_ref,
             lam_ref, dup_ref, dupt_ref, half_ref,
             duy_ref, dcw_ref, dcb_ref, dwa_ref, dba_ref, dwi_ref, dbi_ref, dlam_ref,
             a_s0, a_s1, dh_s, g_s0, g_s1):
        dup = dup_ref[...]
        dup_t = dupt_ref[...]
        same_half = half_ref[...] > 0.5
        taps = _conv_taps(up_ref[...])
        u = cb_ref[...]
        for j in range(4):
            u = u + taps[j] * cw_ref[j:j + 1, :]
        u16 = u.astype(BF16)
        gelu, dgelu = _gelu_and_grad(yb_ref[...])
        dy = dy_ref[...]
        duy_ref[1] = (dy * (hf_ref[...] + hb_ref[...]) * dgelu).astype(BF16)
        dh_s[...] = dy * gelu
        a_s0[...] = _shift_rows(am_ref[0], -1)
        a_s1[...] = _shift_rows(am_ref[2], 1)
        _scans([(a_s0, dh_s, g_s0, True), (a_s1, dh_s, g_s1, False)])
        du = jnp.zeros((T, CG), F32)
        for d, g_s in enumerate((g_s0, g_s1)):
            reverse = d == 1
            wa = _pair_block_diag(wa_ref[d, 0], dup, same_half)
            wi = _pair_block_diag(wi_ref[d, 0], dup, same_half)
            lam_d = lam_ref[d:d + 1, :]
            r = _sigmoid(_dot(u16, wa) + ba_ref[d:d + 1, :])
            ig = _sigmoid(_dot(u16, wi) + bi_ref[d:d + 1, :])
            sp = _softplus(-lam_d)
            a, mult = am_ref[2 * d], am_ref[2 * d + 1]
            mult2 = mult * mult
            g = g_s[...]
            h_prev = _shift_rows(hb_ref[...], -1) if reverse else _shift_rows(hf_ref[...], 1)
            da = g * h_prev
            dmult = g * (ig * u)
            dig = g * mult * u
            du = du + g * mult * ig
            dmult_dlog = jnp.where(mult2 > 0.0, -(a * a) * lax.rsqrt(mult2), 0.0)
            dlog_a = da * a + dmult * dmult_dlog
            dr = dlog_a * ((-LRU_C) * sp)
            dsp = jnp.sum(dlog_a * ((-LRU_C) * r), axis=0, keepdims=True)
            dlam_ref[d:d + 1, :] = dsp * (-_sigmoid(-lam_d))
            dga = dr * r * (1.0 - r)
            dgi = dig * ig * (1.0 - ig)
            dga16 = dga.astype(BF16)
            dgi16 = dgi.astype(BF16)
            du = du + _dot_nt(dga16, wa) + _dot_nt(dgi16, wi)
            dwa_ref[d, 0] = _dot_exact(jnp.where(same_half, _dot_tn(u16, dga16), 0.0), dup_t)
            dwi_ref[d, 0] = _dot_exact(jnp.where(same_half, _dot_tn(u16, dgi16), 0.0), dup_t)
            dba_ref[d:d + 1, :] = jnp.sum(dga, axis=0, keepdims=True)
            dbi_ref[d:d + 1, :] = jnp.sum(dgi, axis=0, keepdims=True)
        dcb_ref[...] = jnp.sum(du, axis=0, keepdims=True)
        for j in range(4):
            dcw_ref[j:j + 1, :] = jnp.sum(du * taps[j], axis=0, keepdims=True)
        dup_in = (_shift_rows(du, -2) * cw_ref[0:1, :] + _shift_rows(du, -1) * cw_ref[1:2, :]
                  + du * cw_ref[2:3, :] + _shift_rows(du, 1) * cw_ref[3:4, :])
        duy_ref[0] = dup_in.astype(BF16)

    wshape = jax.ShapeDtypeStruct((2, N_CG, CG, REC_BLOCK), F32)
    vec = lambda rows: jax.ShapeDtypeStruct((rows, D_REC), F32)
    dup_np = _dup_table()
    return pl.pallas_call(
        body, name="rec_bwd",
        out_shape=(jax.ShapeDtypeStruct((2, T, D_REC), BF16),
                   vec(4), vec(1), wshape, vec(2), wshape, vec(2), vec(2)),
        grid=(N_CG,),
        in_specs=[tok(0), tok(N_CG), tok(0), tok(0), pl.BlockSpec((4, T, CG), lambda g: (0, 0, g)), tok(0),
                  per_ch(4), per_ch(1), wspec, per_ch(2), wspec, per_ch(2), per_ch(2),
                  const((REC_BLOCK, CG)), const((CG, REC_BLOCK)), const((CG, CG))],
        out_specs=(pl.BlockSpec((2, T, CG), lambda g: (0, 0, g)),
                   per_ch(4), per_ch(1), wspec, per_ch(2), wspec, per_ch(2), per_ch(2)),
        scratch_shapes=[pltpu.VMEM((T, CG), F32)] * 5,
        compiler_params=_params(dimension_semantics=("parallel",)),
    )(uy, uy, hf, hb, am, dyrec, conv_w, conv_b, w_a, b_a, w_i, b_i, lam,
      jnp.asarray(dup_np, BF16), jnp.asarray(dup_np.T.copy()), jnp.asarray(_pair_mask()))


TM_MIX = 256


def _mix_specs():
    tok = lambda width, blk=0: pl.BlockSpec((TM_MIX, width), lambda i: (i, blk))
    full = lambda shape: pl.BlockSpec(shape, lambda i: (0, 0))
    return tok, full


def _mix_fwd(x, att, yrec, gg, w_att_o_t, w_rec_o, w_out):
    tok, full = _mix_specs()

    def body(x_ref, att_ref, yr_ref, ga_ref, gr_ref, wao_ref, wro_ref, wo_ref, x1_ref, mixed_ref):
        y_att = _dot_nt(att_ref[...], wao_ref[...])
        y_rec = _dot(yr_ref[...], wro_ref[...])
        mixed = (_sigmoid(ga_ref[...]) * y_att + _sigmoid(gr_ref[...]) * y_rec).astype(BF16)
        mixed_ref[...] = mixed
        x1_ref[...] = x_ref[...] + _dot(mixed, wo_ref[...])

    return pl.pallas_call(
        body, name="mix_fwd",
        out_shape=(jax.ShapeDtypeStruct((T, D), F32), jax.ShapeDtypeStruct((T, D), BF16)),
        grid=(T // TM_MIX,),
        in_specs=[tok(D), tok(D_ATT), tok(D_REC), tok(D, 0), tok(D, 1),
                  full((D, D_ATT)), full((D_REC, D)), full((D, D))],
        out_specs=(tok(D), tok(D)),
        compiler_params=_params(dimension_semantics=("parallel",)),
    )(x, att, yrec, gg, gg, w_att_o_t, w_rec_o, w_out)


def _mix_bwd(dx1, att, yrec, gg, w_att_o_t, w_rec_o, w_out, after):
    tok, full = _mix_specs()

    def body(dx_ref, att_ref, yr_ref, ga_ref, gr_ref, wao_ref, wro_ref, wo_ref, after_ref,
             dgg_ref, dya_ref, dyr_ref, datt_ref, dyrp_ref):
        dmixed = _dot_nt(dx_ref[...].astype(BF16), wo_ref[...])
        y_att = _dot_nt(att_ref[...], wao_ref[...])
        y_rec = _dot(yr_ref[...], wro_ref[...])
        sa = _sigmoid(ga_ref[...])
        sr = _sigmoid(gr_ref[...])
        dgg_ref[0] = (dmixed * y_att * sa * (1.0 - sa)).astype(BF16)
        dgg_ref[1] = (dmixed * y_rec * sr * (1.0 - sr)).astype(BF16)
        dya = (dmixed * sa).astype(BF16)
        dyr = (dmixed * sr).astype(BF16)
        dya_ref[...] = dya
        dyr_ref[...] = dyr
        datt_ref[...] = _dot(dya, wao_ref[...]).astype(BF16)
        dyrp_ref[...] = _dot_nt(dyr, wro_ref[...])

    return pl.pallas_call(
        body, name="mix_bwd",
        out_shape=(jax.ShapeDtypeStruct((2, T, D), BF16),
                   jax.ShapeDtypeStruct((T, D), BF16), jax.ShapeDtypeStruct((T, D), BF16),
                   jax.ShapeDtypeStruct((T, D_ATT), BF16), jax.ShapeDtypeStruct((T, D_REC), F32)),
        grid=(T // TM_MIX,),
        in_specs=[tok(D), tok(D_ATT), tok(D_REC), tok(D, 0), tok(D, 1),
                  full((D, D_ATT)), full((D_REC, D)), full((D, D)), pl.BlockSpec(memory_space=pl.ANY)],
        out_specs=(pl.BlockSpec((2, TM_MIX, D), lambda i: (0, i, 0)),
                   tok(D), tok(D), tok(D_ATT), tok(D_REC)),
        compiler_params=_params(dimension_semantics=("parallel",)),
    )(dx1, att, yrec, gg, gg, w_att_o_t, w_rec_o, w_out, after)


TM_FFN = 256
FF_CHUNK = 1024


def _ffn_loss(x1, target, g2, gf, w_ff1_t, w_ff2):
    n_chunks = D_FF // FF_CHUNK

    def body(x1_ref, tg_ref, g2_ref, gf_ref, w1_hbm, w2_hbm,
             loss_ref, dx1_ref, h2_ref, act_ref, dpre_ref, dx2_ref, dg2_ref, dgf_ref,
             w1, w2, relu_s):
        i = pl.program_id(0)

        @pl.when(i == 0)
        def _():
            pltpu.sync_copy(w1_hbm, w1)
            pltpu.sync_copy(w2_hbm, w2)
            loss_ref[...] = jnp.zeros_like(loss_ref)
            dg2_ref[...] = jnp.zeros_like(dg2_ref)
            dgf_ref[...] = jnp.zeros_like(dgf_ref)

        x1v = x1_ref[...]
        r2 = lax.rsqrt(jnp.mean(x1v * x1v, axis=-1, keepdims=True) + EPS)
        xh2 = x1v * r2
        h2 = (xh2 * g2_ref[...]).astype(BF16)
        h2_ref[...] = h2
        x2 = x1v
        for c in range(n_chunks):
            ff = slice(c * FF_CHUNK, (c + 1) * FF_CHUNK)
            rl = jnp.maximum(_dot_nt(h2, w1[ff, :]), 0.0)
            relu_s[:, ff] = rl
            act = (rl * rl).astype(BF16)
            act_ref[:, ff] = act
            x2 = x2 + _dot(act, w2[ff, :])
        r3 = lax.rsqrt(jnp.mean(x2 * x2, axis=-1, keepdims=True) + EPS)
        xh3 = x2 * r3
        err = xh3 * gf_ref[...] - tg_ref[...]
        loss_ref[...] += 0.5 * jnp.sum(jnp.mean(err * err, axis=-1, keepdims=True))
        dy = err * (1.0 / D)
        dgf_ref[...] += jnp.sum(dy * xh3, axis=0, keepdims=True)
        dx2 = _rms_bwd(dy, xh3, r3, gf_ref[...])
        dx2_16 = dx2.astype(BF16)
        dx2_ref[...] = dx2_16
        dh2 = jnp.zeros((TM_FFN, D), F32)
        for c in range(n_chunks):
            ff = slice(c * FF_CHUNK, (c + 1) * FF_CHUNK)
            dpre = (_dot_nt(dx2_16, w2[ff, :]) * (2.0 * relu_s[:, ff])).astype(BF16)
            dpre_ref[:, ff] = dpre
            dh2 = dh2 + _dot(dpre, w1[ff, :])
        dg2_ref[...] += jnp.sum(dh2 * xh2, axis=0, keepdims=True)
        dx1_ref[...] = dx2 + _rms_bwd(dh2, xh2, r2, g2_ref[...])

    tok = lambda width: pl.BlockSpec((TM_FFN, width), lambda i: (i, 0))
    vec = pl.BlockSpec((1, D), lambda i: (0, 0))
    hbm = pl.BlockSpec(memory_space=pl.ANY)
    return pl.pallas_call(
        body, name="ffn_loss",
        out_shape=(jax.ShapeDtypeStruct((8, 128), F32), jax.ShapeDtypeStruct((T, D), F32),
                   jax.ShapeDtypeStruct((T, D), BF16), jax.ShapeDtypeStruct((T, D_FF), BF16),
                   jax.ShapeDtypeStruct((T, D_FF), BF16), jax.ShapeDtypeStruct((T, D), BF16),
                   jax.ShapeDtypeStruct((1, D), F32), jax.ShapeDtypeStruct((1, D), F32)),
        grid=(T // TM_FFN,),
        in_specs=[tok(D), tok(D), vec, vec, hbm, hbm],
        out_specs=(pl.BlockSpec((8, 128), lambda i: (0, 0)), tok(D), tok(D), tok(D_FF), tok(D_FF), tok(D),
                   vec, vec),
        scratch_shapes=[pltpu.VMEM((D_FF, D), BF16), pltpu.VMEM((D_FF, D), BF16),
                        pltpu.VMEM((TM_FFN, D_FF), F32)],
        compiler_params=_params(dimension_semantics=("arbitrary",)),
    )(x1, target, g2, gf, w_ff1_t, w_ff2)


def _local_step(x, target, p, late_weights, reduce_first, reduce_early, reduce_late):
    bias = _rpb_rows(p["rpb"])
    pairs = lambda w: w.reshape(2, N_CG, CG, REC_BLOCK)
    w_a, w_i = pairs(p["w_rg_a"]), pairs(p["w_rg_i"])
    rec_params = (p["conv_w"], p["conv_b"], w_a, p["b_rg_a"], w_i, p["b_rg_i"], p["lru_lambda"])

    qkv, uy, gg, h = _in_proj(x, p["ln1_g"], p["w_in_t"], p["b_in"])
    att = _att_fwd(qkv, bias)
    hf, hb, yrec, am = _rec_fwd(uy, *rec_params)
    p = {**p, **late_weights(yrec, 0)}
    x1, mixed = _mix_fwd(x, att, yrec, gg, p["w_att_o_t"], p["w_rec_o"], p["w_out"])
    p = {**p, **late_weights(x1, 1)}
    loss8, dx1, h2, act, dpre, dx2, g_ln2, g_lnf = _ffn_loss(
        x1, target, p["ln2_g"], p["lnf_g"], p["w_ff1_t"], p["w_ff2"])

    grads = {"ln2_g": g_ln2, "lnf_g": g_lnf,
             "w_ff1_t": _matmul(dpre, h2, "tn", BF16, "g_w_ff1"),
             "w_ff2": _matmul(act, dx2, "tn", BF16, "g_w_ff2")}
    dgg, dya, dyr, datt, dyrp = _mix_bwd(dx1, att, yrec, gg, p["w_att_o_t"], p["w_rec_o"], p["w_out"],
                                         reduce_first(grads, None))
    lam_after = rec_params[-1] + reduce_first(None, dgg)[0, 0]
    duy, g_cw, g_cb, g_wa, g_ba, g_wi, g_bi, g_lam = _rec_bwd(uy, hf, hb, am, dyrp, *rec_params[:-1], lam_after)
    blocks = lambda g: g.reshape(2, N_REC_BLOCKS, REC_BLOCK, REC_BLOCK)
    grads.update({
        "w_att_o_t": _matmul(dya, att, "tn", BF16, "g_w_att_o"),
        "conv_w": g_cw, "conv_b": g_cb, "w_rg_a": blocks(g_wa), "b_rg_a": g_ba,
        "w_rg_i": blocks(g_wi), "b_rg_i": g_bi, "lru_lambda": g_lam,
        "w_rec_o": _matmul(yrec, dyr, "tn", BF16, "g_w_rec_o"),
        "w_out": _matmul(mixed, dx1, "tn", BF16, "g_w_out"),
    })
    dqkv, gbias = _att_bwd(qkv, bias, datt, reduce_early(grads))
    dz = (dqkv, duy, dgg)
    g_w_in_t, g_b_in = _grad_w_in(dz, h)
    grads.update(w_in_t=g_w_in_t, b_in=g_b_in)
    grad_x, g_ln1 = _dh_norm1_bwd(dz, p["w_in_t"], x, p["ln1_g"], dx1, reduce_late(grads))
    grads.update(ln1_g=g_ln1, rpb=_rpb_fold(gbias))
    return loss8[0:1, 0:1], grad_x, grads


MESH_ID = pl.DeviceIdType.MESH
ANY = pl.BlockSpec(memory_space=pl.ANY)

CHAN_BLOCK_ROWS = 32
GATE_ROWS = 2 * 2 * N_REC_BLOCKS * REC_BLOCK * REC_BLOCK // (N_DEV * D)
SECTIONS = (("w_in_t", 704, D), ("w_rec_o", 128, D), ("w_out", 128, D), ("w_ff1_t", 512, D),
            ("w_ff2", 512, D), ("chan", CHAN_BLOCK_ROWS, D), ("w_att_o_t", 128, D_ATT),
            ("gates", GATE_ROWS, D))
N_SEC = len(SECTIONS)
N_CHAN_ROWS = 10
CHAN = (("conv_w", 4), ("b_rg_a", 2), ("b_rg_i", 2), ("lru_lambda", 2))


def _position():
    return lax.axis_index("x"), lax.axis_index("y"), lax.axis_index("c")


def _other_chips(x, y):
    return [(1 - x, y), (x, 1 - y), (1 - x, 1 - y)]


PASS_ON_IDS, PAIR_EARLY_ID, PAIR_LATE_ID, PAIR_FIRST_ID = (1, 4), 2, 3, 5


def _pair_handshake(x, y, c):
    barrier = pltpu.get_barrier_semaphore()
    pl.semaphore_signal(barrier, inc=1, device_id=(x, y, 1 - c), device_id_type=MESH_ID)
    pl.semaphore_wait(barrier, 1)


def _block_of(ref, dev, rows):
    return ref.at[pl.ds(pl.multiple_of(dev * rows, 16), rows)]


def _all_gather(shards, name):
    ns = len(shards)

    def body(*refs):
        x_refs, out_refs, done_ref = refs[:ns], refs[ns:2 * ns], refs[2 * ns]
        send_sems, recv_sems, local_sems = refs[2 * ns + 1:]
        done_ref[0, 0] = 0.0
        x, y, c = _position()
        me, sibling = (x, y, c), (x, y, 1 - c)
        x_nbr, y_nbr, diagonal = _other_chips(x, y)
        north = c == 1
        relay_from = (jnp.where(north, x_nbr[0], y_nbr[0]), jnp.where(north, x_nbr[1], y_nbr[1]))
        relay_to = (jnp.where(north, y_nbr[0], x_nbr[0]), jnp.where(north, y_nbr[1], x_nbr[1]))

        def rows(s, px, py, pc):
            return _block_of(out_refs[s], 4 * px + 2 * py + pc, shards[s].shape[0])

        def copy(k, s, block, to, from_shard=False):
            return pltpu.make_async_remote_copy(
                src_ref=x_refs[s] if from_shard else rows(s, *block), dst_ref=rows(s, *block),
                send_sem=send_sems.at[k * ns + s], recv_sem=recv_sems.at[k * ns + s],
                device_id=to, device_id_type=MESH_ID)

        sections = range(ns)
        mine = [pltpu.make_async_copy(x_refs[s], rows(s, *me), local_sems.at[s]) for s in sections]
        sent = [copy(k, s, me, to, True) for k, to in enumerate((sibling, (*x_nbr, c), (*y_nbr, c)))
                for s in sections]
        for cp in mine + sent:
            cp.start()
        for s in sections:
            copy(1, s, (*x_nbr, c), me).wait_recv()
            copy(2, s, (*y_nbr, c), me).wait_recv()
            sent += [copy(3, s, (*relay_from, c), (*relay_to, c)),
                     copy(4, s, (*x_nbr, c), sibling), copy(5, s, (*y_nbr, c), sibling)]
            for cp in sent[-3:]:
                cp.start()
        for s in sections:
            copy(3, s, (*diagonal, c), me).wait_recv()
            sent.append(copy(6, s, (*diagonal, c), sibling))
            sent[-1].start()
        for s in sections:
            copy(0, s, sibling, me).wait_recv()
            for k, chip in ((4, x_nbr), (5, y_nbr), (6, diagonal)):
                copy(k, s, (*chip, 1 - c), me).wait_recv()
        for cp in sent:
            cp.wait_send()
        for cp in mine:
            cp.wait()

    return pl.pallas_call(
        body, name=name,
        out_shape=tuple(jax.ShapeDtypeStruct((N_DEV * s.shape[0], s.shape[1]), s.dtype) for s in shards)
        + (jax.ShapeDtypeStruct((1, 1), F32),),
        in_specs=[ANY] * ns,
        out_specs=(ANY,) * ns + (pl.BlockSpec(memory_space=pltpu.SMEM),),
        scratch_shapes=[pltpu.SemaphoreType.DMA((7 * ns,)), pltpu.SemaphoreType.DMA((7 * ns,)),
                        pltpu.SemaphoreType.DMA((ns,))],
    )(*shards)


HBM = pl.BlockSpec(memory_space=pltpu.HBM)
SEM = pl.BlockSpec(memory_space=pltpu.SEMAPHORE)
EFFECT = pltpu.SideEffectType.DATAFLOW_SIDE_EFFECTING


def _in_hbm(a):
    return pltpu.with_memory_space_constraint(a, pltpu.HBM)


def _first_hop_copies(shards, x_refs, zones, send_sems, recv_sems):
    ns = len(shards)
    x, y, c = _position()
    targets = [(x, y, 1 - c)] + [(cx, cy, c) for cx, cy in _other_chips(x, y)]
    return [pltpu.make_async_remote_copy(
        src_ref=x_refs[s], dst_ref=_block_of(zones[s], 4 * x + 2 * y + c, shards[s].shape[0]),
        send_sem=send_sems.at[k * ns + s], recv_sem=recv_sems.at[k * ns + s],
        device_id=to, device_id_type=MESH_ID)
        for k, to in enumerate(targets) for s in range(ns)]


def _after_all(arrays, name):
    def body(*refs):
        refs[-1][...] = jnp.zeros_like(refs[-1])

    return pl.pallas_call(
        body, name=name,
        out_shape=jax.ShapeDtypeStruct((8, LANES), F32),
        in_specs=[pl.BlockSpec(memory_space=pl.ANY)] * len(arrays),
        out_specs=pl.BlockSpec(memory_space=pltpu.VMEM),
    )(*arrays)


def _own_blocks_placed(shards, after):
    ns = len(shards)
    x, y, c = _position()
    me = jnp.reshape(4 * x + 2 * y + c, (1,)).astype(jnp.int32)
    shards = [*shards[:-1], shards[-1] + after.astype(shards[-1].dtype)]

    def body(me_ref, *refs):
        for s in range(ns):
            refs[ns + s][...] = refs[s][...]

    return pl.pallas_call(
        body, name="own_blocks_placed",
        out_shape=tuple(jax.ShapeDtypeStruct((N_DEV * s.shape[0], s.shape[1]), s.dtype) for s in shards),
        grid_spec=pltpu.PrefetchScalarGridSpec(
            num_scalar_prefetch=1, grid=(1,),
            in_specs=[pl.BlockSpec(s.shape, lambda i, me: (0, 0)) for s in shards],
            out_specs=tuple(pl.BlockSpec(s.shape, lambda i, me: (me[0], 0)) for s in shards)),
        compiler_params=_params(dimension_semantics=("arbitrary",)),
    )(me, *shards)


def _gather_start(shards, after, name):
    ns = len(shards)
    zones = _own_blocks_placed(shards, after)

    def body(*refs):
        for cp in _first_hop_copies(shards, refs[:ns], refs[ns:2 * ns], refs[2 * ns], refs[2 * ns + 1]):
            cp.start()
        refs[-1][...] = jnp.zeros_like(refs[-1])

    out = pl.pallas_call(
        body, name=name,
        out_shape=(pltpu.SemaphoreType.DMA((4 * ns,)), pltpu.SemaphoreType.DMA((4 * ns,)),
                   *[pltpu.HBM(a.shape, a.dtype) for a in (*shards, *zones)],
                   jax.ShapeDtypeStruct((8, LANES), F32)),
        in_specs=[HBM] * (2 * ns),
        out_specs=(SEM, SEM, *[HBM] * (2 * ns), pl.BlockSpec(memory_space=pltpu.VMEM)),
        input_output_aliases={i: 2 + i for i in range(2 * ns)},
        compiler_params=pltpu.CompilerParams(has_side_effects=EFFECT),
    )(*[_in_hbm(a) for a in shards], *[_in_hbm(a) for a in zones])
    return out[0], out[1], out[2:2 + ns], out[2 + ns:2 + 2 * ns], out[-1]


def _gather_wait(send_sems, recv_sems, shards, zones, which, after, name):
    ns = len(shards)

    def body(*refs):
        copies = _first_hop_copies(shards, refs[:ns], refs[ns:2 * ns], refs[2 * ns], refs[2 * ns + 1])
        for i, cp in enumerate(copies):
            if i % ns in which:
                cp.wait_send()
                cp.wait_recv()

    out = pl.pallas_call(
        body, name=name,
        out_shape=tuple(pltpu.HBM(a.shape, a.dtype) for a in (*shards, *zones)),
        in_specs=[HBM] * (2 * ns) + [SEM, SEM, ANY],
        out_specs=(HBM,) * (2 * ns),
        input_output_aliases={i: i for i in range(2 * ns)},
        compiler_params=pltpu.CompilerParams(has_side_effects=EFFECT),
    )(*shards, *zones, send_sems, recv_sems, after)
    return out[:ns], out[ns:]


def _gather_pass_on(rows, zones, barrier_id, name):
    ns = len(zones)

    def body(*refs):
        in_refs, out_refs = refs[:ns], refs[ns:2 * ns]
        send_sems, recv_sems = refs[2 * ns:]
        x, y, c = _position()
        _pair_handshake(x, y, c)
        copies = [pltpu.make_async_remote_copy(
            src_ref=_block_of(in_refs[s], 4 * cx + 2 * cy + c, rows[s]),
            dst_ref=_block_of(out_refs[s], 4 * cx + 2 * cy + c, rows[s]),
            send_sem=send_sems.at[j * ns + s], recv_sem=recv_sems.at[j * ns + s],
            device_id=(x, y, 1 - c), device_id_type=MESH_ID)
            for j, (cx, cy) in enumerate(_other_chips(x, y)) for s in range(ns)]
        for cp in copies:
            cp.start()
        for cp in copies:
            cp.wait_recv()
        for cp in copies:
            cp.wait_send()

    return pl.pallas_call(
        body, name=name,
        out_shape=tuple(jax.ShapeDtypeStruct(z.shape, z.dtype) for z in zones),
        in_specs=[ANY] * ns, out_specs=(ANY,) * ns,
        input_output_aliases={i: i for i in range(ns)},
        scratch_shapes=[pltpu.SemaphoreType.DMA((3 * ns,)), pltpu.SemaphoreType.DMA((3 * ns,))],
        compiler_params=pltpu.CompilerParams(collective_id=barrier_id),
    )(*zones)


def _pair_copies(sections, g_refs, land, send_sems, recv_sems):
    ns = len(sections)
    x, y, c = _position()
    return [pltpu.make_async_remote_copy(
        src_ref=_block_of(g_refs[s], 2 * k + 1 - c, rows), dst_ref=land[s].at[k],
        send_sem=send_sems.at[k * ns + s], recv_sem=recv_sems.at[k * ns + s],
        device_id=(x, y, 1 - c), device_id_type=MESH_ID)
        for k in range(N_CHIPS) for s, (_, rows, _) in enumerate(sections)]


def _pair_exchange_start(sections, grads, barrier_id, name):
    ns = len(sections)

    def body(*refs):
        _pair_handshake(*_position())
        for cp in _pair_copies(sections, refs[:ns], refs[ns:2 * ns], refs[2 * ns], refs[2 * ns + 1]):
            cp.start()
        refs[-1][...] = jnp.zeros_like(refs[-1])

    zones = [lax.empty((N_CHIPS, rows, cols), BF16) for _, rows, cols in sections]
    n = N_CHIPS * ns
    out = pl.pallas_call(
        body, name=name,
        out_shape=(pltpu.SemaphoreType.DMA((n,)), pltpu.SemaphoreType.DMA((n,)),
                   *[pltpu.HBM(a.shape, a.dtype) for a in (*grads, *zones)],
                   jax.ShapeDtypeStruct((8, LANES), F32)),
        in_specs=[HBM] * (2 * ns),
        out_specs=(SEM, SEM, *[HBM] * (2 * ns), pl.BlockSpec(memory_space=pltpu.VMEM)),
        input_output_aliases={i: 2 + i for i in range(2 * ns)},
        compiler_params=pltpu.CompilerParams(has_side_effects=EFFECT, collective_id=barrier_id),
    )(*[_in_hbm(a) for a in grads], *[_in_hbm(a) for a in zones])
    return out[0], out[1], out[2:2 + ns], out[2 + ns:2 + 2 * ns], out[-1]


def _pair_exchange_wait(sections, send_sems, recv_sems, grads, zones, after, name):
    ns = len(sections)

    def body(*refs):
        for cp in _pair_copies(sections, refs[:ns], refs[ns:2 * ns], refs[2 * ns], refs[2 * ns + 1]):
            cp.wait_send()
            cp.wait_recv()

    out = pl.pallas_call(
        body, name=name,
        out_shape=tuple(pltpu.HBM(a.shape, a.dtype) for a in (*grads, *zones)),
        in_specs=[HBM] * (2 * ns) + [SEM, SEM, ANY],
        out_specs=(HBM,) * (2 * ns),
        input_output_aliases={i: i for i in range(2 * ns)},
        compiler_params=pltpu.CompilerParams(has_side_effects=EFFECT),
    )(*grads, *zones, send_sems, recv_sems, after)
    return out[:ns], out[ns:]


def _pair_add(sections, grads, got, core, name):
    ns = len(sections)

    def body(core_ref, *refs):
        g_refs, got_refs, p_refs = refs[:ns], refs[ns:2 * ns], refs[2 * ns:]
        for s in range(ns):
            p_refs[s][0] = (g_refs[s][...].astype(F32) + got_refs[s][0].astype(F32)).astype(BF16)

    slot = [pl.BlockSpec((1, rows, cols), lambda k, c: (k, 0, 0)) for _, rows, cols in sections]
    return pl.pallas_call(
        body, name=name,
        out_shape=tuple(jax.ShapeDtypeStruct((N_CHIPS, rows, cols), BF16) for _, rows, cols in sections),
        grid_spec=pltpu.PrefetchScalarGridSpec(
            num_scalar_prefetch=1, grid=(N_CHIPS,),
            in_specs=[pl.BlockSpec((rows, cols), lambda k, c: (2 * k + c[0], 0)) for _, rows, cols in sections]
            + slot,
            out_specs=tuple(slot)),
        compiler_params=_params(dimension_semantics=("parallel",)),
    )(core, *grads, *got)


def _chip_copies(sections, p_refs, land, send_sems, recv_sems):
    ns = len(sections)
    x, y, c = _position()
    return [pltpu.make_async_remote_copy(
        src_ref=p_refs[s].at[2 * cx + cy], dst_ref=land[s].at[j],
        send_sem=send_sems.at[j * ns + s], recv_sem=recv_sems.at[j * ns + s],
        device_id=(cx, cy, c), device_id_type=MESH_ID)
        for j, (cx, cy) in enumerate(_other_chips(x, y)) for s in range(ns)]


def _chip_exchange(sections, parts, name):
    ns = len(sections)

    def body(*refs):
        copies = _chip_copies(sections, refs[:ns], refs[ns:2 * ns], *refs[2 * ns:])
        for cp in copies:
            cp.start()
        for cp in copies:
            cp.wait_recv()
        for cp in copies:
            cp.wait_send()

    n = 3 * ns
    return pl.pallas_call(
        body, name=name,
        out_shape=tuple(jax.ShapeDtypeStruct((3, rows, cols), BF16) for _, rows, cols in sections),
        in_specs=[ANY] * ns, out_specs=(ANY,) * ns,
        scratch_shapes=[pltpu.SemaphoreType.DMA((n,)), pltpu.SemaphoreType.DMA((n,))],
    )(*parts)


def _chip_exchange_start(sections, parts, name):
    ns = len(sections)

    def body(*refs):
        p_refs, land = refs[:ns], refs[ns:2 * ns]
        send_sems, recv_sems = refs[2 * ns], refs[2 * ns + 1]
        token = refs[-1]
        for cp in _chip_copies(sections, p_refs, land, send_sems, recv_sems):
            cp.start()
        token[...] = jnp.zeros_like(token)

    zones = [lax.empty((3, rows, cols), BF16) for _, rows, cols in sections]
    out = pl.pallas_call(
        body, name=name,
        out_shape=(pltpu.SemaphoreType.DMA((3 * ns,)), pltpu.SemaphoreType.DMA((3 * ns,)),
                   *[pltpu.HBM(a.shape, a.dtype) for a in parts], *[pltpu.HBM(a.shape, a.dtype) for a in zones],
                   jax.ShapeDtypeStruct((8, LANES), F32)),
        in_specs=[HBM] * (2 * ns),
        out_specs=(SEM, SEM, *[HBM] * (2 * ns), pl.BlockSpec(memory_space=pltpu.VMEM)),
        input_output_aliases={i: 2 + i for i in range(2 * ns)},
        compiler_params=pltpu.CompilerParams(has_side_effects=EFFECT),
    )(*[_in_hbm(a) for a in parts], *[_in_hbm(a) for a in zones])
    return out[0], out[1], out[2:2 + ns], out[2 + ns:2 + 2 * ns], out[-1]


def _chip_exchange_wait(sections, send_sems, recv_sems, parts, zones, after, name):
    ns = len(sections)

    def body(*refs):
        p_refs, land = refs[:ns], refs[ns:2 * ns]
        for cp in _chip_copies(sections, p_refs, land, refs[2 * ns], refs[2 * ns + 1]):
            cp.wait_send()
            cp.wait_recv()

    out = pl.pallas_call(
        body, name=name,
        out_shape=tuple(pltpu.HBM(a.shape, a.dtype) for a in (*parts, *zones)),
        in_specs=[HBM] * (2 * ns) + [SEM, SEM, ANY],
        out_specs=(HBM,) * (2 * ns),
        input_output_aliases={i: i for i in range(2 * ns)},
        compiler_params=pltpu.CompilerParams(has_side_effects=EFFECT),
    )(*parts, *zones, send_sems, recv_sems, after)
    return out[:ns], out[ns:]


def _grad_finish(sections, parts, far, chip, name):
    ns = len(sections)

    def body(chip_ref, *refs):
        p_refs, b_refs, g_refs = refs[:ns], refs[ns:2 * ns], refs[2 * ns:]
        for s in range(ns):
            g = p_refs[s][0].astype(F32)
            for j in range(3):
                g = g + b_refs[s][j].astype(F32)
            g_refs[s][...] = g

    half = [(rows // 2, cols) for _, rows, cols in sections]
    return pl.pallas_call(
        body, name=name,
        out_shape=tuple(jax.ShapeDtypeStruct((rows, cols), F32) for _, rows, cols in sections),
        grid_spec=pltpu.PrefetchScalarGridSpec(
            num_scalar_prefetch=1, grid=(2,),
            in_specs=[pl.BlockSpec((1, r, c), lambda i, chip: (chip[0], i, 0)) for r, c in half]
            + [pl.BlockSpec((3, r, c), lambda i, chip: (0, i, 0)) for r, c in half],
            out_specs=tuple(pl.BlockSpec((r, c), lambda i, chip: (i, 0)) for r, c in half)),
        compiler_params=_params(dimension_semantics=("parallel",)),
    )(chip, *parts, *far)


def _sum_devices(parts, rows, name):
    cols = parts.shape[1]
    tr = rows // 2

    def body(*refs):
        s = refs[0][...].astype(F32)
        for d in range(1, N_DEV):
            s = s + refs[d][...].astype(F32)
        refs[N_DEV][...] = s

    return pl.pallas_call(
        body, name=name,
        out_shape=jax.ShapeDtypeStruct((rows, cols), F32),
        grid=(2,),
        in_specs=[pl.BlockSpec((tr, cols), lambda i, d=d: (2 * d + i, 0)) for d in range(N_DEV)],
        out_specs=pl.BlockSpec((tr, cols), lambda i: (i, 0)),
        compiler_params=_params(dimension_semantics=("parallel",)),
    )(*([parts] * N_DEV))


def _adamw_step(w_ref, g_ref, m_ref, v_ref, d_ref, nm_ref, nv_ref):
    c1 = 1.0 / (1.0 - ADAM_B1 ** ADAM_STEP)
    c2 = 1.0 / (1.0 - ADAM_B2 ** ADAM_STEP)
    gv = g_ref[...]
    nm = ADAM_B1 * m_ref[...] + (1.0 - ADAM_B1) * gv
    nv = ADAM_B2 * v_ref[...] + (1.0 - ADAM_B2) * (gv * gv)
    nm_ref[...] = nm
    nv_ref[...] = nv
    d_ref[...] = (-ADAM_LR) * ((nm * c1) / (jnp.sqrt(nv * c2) + ADAM_EPS) + ADAM_WD * w_ref[...])


def _adamw_small(params, name):
    n = len(params)

    def body(*refs):
        for k in range(n):
            _adamw_step(*refs[4 * k:4 * k + 4], *refs[4 * n + 3 * k:4 * n + 3 * k + 3])

    out = pl.pallas_call(
        body, name=name,
        out_shape=tuple(jax.ShapeDtypeStruct(p[0].shape, F32) for p in params for _ in range(3)),
    )(*[a for p in params for a in p])
    return [out[3 * k:3 * k + 3] for k in range(n)]


def _adamw(w, g, m, v, name):
    rows, cols = w.shape
    tr = rows
    while tr * cols * 4 > (1 << 20) and tr % 16 == 0:
        tr //= 2

    def body(*refs):
        _adamw_step(*refs)

    spec = pl.BlockSpec((tr, cols), lambda i: (i, 0))
    shape = jax.ShapeDtypeStruct((rows, cols), F32)
    return pl.pallas_call(
        body, name=name,
        out_shape=(shape, shape, shape),
        grid=(rows // tr,),
        in_specs=[spec] * 4, out_specs=(spec,) * 3,
        compiler_params=_params(dimension_semantics=("parallel",)),
    )(w, g, m, v)


NAMES = ("ln1_g", "w_in", "b_in", "rpb", "w_att_o", "conv_w", "conv_b", "w_rg_a", "b_rg_a", "w_rg_i",
         "b_rg_i", "lru_lambda", "w_rec_o", "w_out", "ln2_g", "w_ff1", "w_ff2", "lnf_g")
TRANSPOSED = {"w_in": "w_in_t", "w_att_o": "w_att_o_t", "w_ff1": "w_ff1_t"}
ROW_SHARDED = ("w_rec_o", "w_out", "w_ff2")
REPLICATED = (("ln1_g", (1, D)), ("b_in", (1, D_IN)), ("rpb", (N_HEADS * N_RPB_R, N_RPB_C)),
              ("conv_b", (1, D_REC)), ("w_rg_a", (2 * N_REC_BLOCKS * REC_BLOCK, REC_BLOCK)),
              ("w_rg_i", (2 * N_REC_BLOCKS * REC_BLOCK, REC_BLOCK)), ("ln2_g", (1, D)), ("lnf_g", (1, D)))
GATE_BLOCKS = ("w_rg_a", "w_rg_i")
SMALL_ROWS = 112


def _chan_bits(vectors):
    chan = jnp.concatenate(vectors, axis=0)
    bits = lax.bitcast_convert_type(chan, BF16).reshape(-1)
    return jnp.pad(bits, (0, CHAN_BLOCK_ROWS * D - bits.shape[0])).reshape(CHAN_BLOCK_ROWS, D)


def _chan_from_bits(gathered):
    bits = gathered.reshape(N_DEV, CHAN_BLOCK_ROWS * D)[:, :2 * N_CHAN_ROWS * LANES]
    chan = lax.bitcast_convert_type(bits.reshape(N_DEV, N_CHAN_ROWS, LANES, 2), F32)
    return chan.transpose(1, 0, 2).reshape(N_CHAN_ROWS, D)


def kernel(x, ln1_g, w_in, b_in, rpb, w_att_o, conv_w, conv_b, w_rg_a, b_rg_a, w_rg_i, b_rg_i, lru_lambda, w_rec_o, w_out, ln2_g, w_ff1, w_ff2, lnf_g, loss_target, m_ln1_g, m_w_in, m_b_in, m_rpb, m_w_att_o, m_conv_w, m_conv_b, m_w_rg_a, m_b_rg_a, m_w_rg_i, m_b_rg_i, m_lru_lambda, m_w_rec_o, m_w_out, m_ln2_g, m_w_ff1, m_w_ff2, m_lnf_g, v_ln1_g, v_w_in, v_b_in, v_rpb, v_w_att_o, v_conv_w, v_conv_b, v_w_rg_a, v_b_rg_a, v_w_rg_i, v_b_rg_i, v_lru_lambda, v_w_rec_o, v_w_out, v_ln2_g, v_w_ff1, v_w_ff2, v_lnf_g):
    w = dict(zip(NAMES, (ln1_g, w_in, b_in, rpb, w_att_o, conv_w, conv_b, w_rg_a, b_rg_a, w_rg_i,
                         b_rg_i, lru_lambda, w_rec_o, w_out, ln2_g, w_ff1, w_ff2, lnf_g)))
    m = dict(zip(NAMES, (m_ln1_g, m_w_in, m_b_in, m_rpb, m_w_att_o, m_conv_w, m_conv_b, m_w_rg_a,
                         m_b_rg_a, m_w_rg_i, m_b_rg_i, m_lru_lambda, m_w_rec_o, m_w_out, m_ln2_g,
                         m_w_ff1, m_w_ff2, m_lnf_g)))
    v = dict(zip(NAMES, (v_ln1_g, v_w_in, v_b_in, v_rpb, v_w_att_o, v_conv_w, v_conv_b, v_w_rg_a,
                         v_b_rg_a, v_w_rg_i, v_b_rg_i, v_lru_lambda, v_w_rec_o, v_w_out, v_ln2_g,
                         v_w_ff1, v_w_ff2, v_lnf_g)))
    xi, yi, ci = _position()

    shard = {t: w[n][0].T.astype(BF16) for n, t in TRANSPOSED.items()}
    shard.update({n: w[n][0].astype(BF16) for n in ROW_SHARDED})
    shard["chan"] = _chan_bits([w[n][0] for n, _ in CHAN])
    first, later = ("w_in_t", "chan"), ("w_rec_o", "w_out", "w_att_o_t", "w_ff1_t", "w_ff2")
    *gathered, done = _all_gather([shard[n] for n in first], "weight_all_gather")
    p = dict(zip(first, gathered))
    send_sems, recv_sems, sent, zones, token = _gather_start([shard[n] for n in later], done,
                                                             "weight_gather_start")

    travelling = {"shards": sent, "zones": zones}
    stages = (("w_rec_o", "w_out", "w_att_o_t"), ("w_ff1_t", "w_ff2"))

    def late_weights(after, stage):
        which = [later.index(n) for n in stages[stage]]
        travelling["shards"], travelling["zones"] = _gather_wait(
            send_sems, recv_sems, travelling["shards"], travelling["zones"], which, after,
            "weight_gather_wait_%d" % stage)
        return dict(zip(stages[stage], _gather_pass_on(
            [shard[n].shape[0] for n in stages[stage]], [travelling["zones"][i] for i in which],
            PASS_ON_IDS[stage], "weight_gather_pass_on_%d" % stage)))

    chan = _chan_from_bits(p.pop("chan"))
    r0 = 0
    for n, rows in CHAN:
        p[n] = chan[r0:r0 + rows]
        r0 += rows
    p.update(ln1_g=w["ln1_g"], b_in=w["b_in"] + token[0, 0], rpb=w["rpb"][0], conv_b=w["conv_b"],
             w_rg_a=w["w_rg_a"][0], w_rg_i=w["w_rg_i"][0], ln2_g=w["ln2_g"],
             lnf_g=w["lnf_g"].reshape(1, D))

    core = jnp.reshape(ci, (1,)).astype(jnp.int32)
    chip = jnp.reshape(2 * xi + yi, (1,)).astype(jnp.int32)
    first_sections = tuple(s for s in SECTIONS if s[0] in ("w_ff1_t", "w_ff2"))
    late_sections = SECTIONS[:1]
    early_sections = tuple(s for s in SECTIONS[1:] if s not in first_sections)
    in_flight = {}

    def pair_sum_and_send(group, sections, after):
        send_sems, recv_sems, sect, zones, _ = in_flight["pair_" + group]
        sect, got = _pair_exchange_wait(sections, send_sems, recv_sems, sect, zones, after,
                                        "grad_pair_exchange_wait_" + group)
        parts = _pair_add(sections, sect, got, core, "grad_pair_add_" + group)
        in_flight[group] = _chip_exchange_start(sections, parts, "grad_chip_exchange_start_" + group)
        return in_flight[group][-1]

    def pair_exchange_at_once(group, sections, grads, barrier_id):
        in_flight["pair_" + group] = _pair_exchange_start(
            sections, [grads[n] for n, _, _ in sections], barrier_id, "grad_pair_exchange_start_" + group)
        return pair_sum_and_send(group, sections, in_flight["pair_" + group][-1])

    def reduce_first(grads, after):
        if grads is None:
            return pair_sum_and_send("first", first_sections, after)
        in_flight["pair_first"] = _pair_exchange_start(
            first_sections, [grads[n] for n, _, _ in first_sections], PAIR_FIRST_ID,
            "grad_pair_exchange_start_first")
        return in_flight["pair_first"][-1]

    def reduce_early(grads):
        chan_g = jnp.concatenate([grads[n] for n, _ in CHAN], axis=0)
        chan_g = chan_g.reshape(N_CHAN_ROWS, N_DEV, LANES).transpose(1, 0, 2).astype(BF16)
        chan_g = jnp.pad(chan_g.reshape(N_DEV, -1), ((0, 0), (0, CHAN_BLOCK_ROWS * D - N_CHAN_ROWS * LANES)))
        grads["chan"] = chan_g.reshape(N_DEV * CHAN_BLOCK_ROWS, D)
        grads["gates"] = jnp.concatenate([grads[n].reshape(-1, D) for n in GATE_BLOCKS], axis=0).astype(BF16)
        return pair_exchange_at_once("early", early_sections, grads, PAIR_EARLY_ID)[0, 0]

    def finish(group, sections, after, name):
        send_sems, recv_sems, parts, zones, _ = in_flight[group]
        parts, far = _chip_exchange_wait(sections, send_sems, recv_sems, parts, zones, after,
                                         "grad_chip_exchange_wait_" + name)
        return dict(zip((n for n, _, _ in sections),
                        _grad_finish(sections, parts, far, chip, "grad_finish_" + name)))

    summed = {}

    def reduce_late(grads):
        in_flight["pair_late"] = _pair_exchange_start(
            late_sections, [grads[n] for n, _, _ in late_sections], PAIR_LATE_ID,
            "grad_pair_exchange_start_late")
        summed.update(finish("first", first_sections, in_flight["pair_late"][-1], "first"))
        summed.update(finish("early", early_sections, summed["w_ff2"], "early"))
        return pair_sum_and_send("late", late_sections, summed["gates"])

    loss_part, grad_x, grads = _local_step(x[0], loss_target[0], p, late_weights, reduce_first, reduce_early,
                                           reduce_late)

    flat = jnp.concatenate([grads[n].reshape(-1) for n, _ in REPLICATED if n not in GATE_BLOCKS]
                           + [loss_part.reshape(-1)])
    n_small = flat.shape[0]
    flat = jnp.pad(flat, (0, SMALL_ROWS * LANES - n_small)).reshape(SMALL_ROWS, LANES)
    small_parts, gate_sum, _ = _all_gather([flat, summed["gates"]], "small_grad_all_gather")
    small = _sum_devices(small_parts, SMALL_ROWS, "small_grad_sum").reshape(-1)
    loss = small[n_small - 1]

    g, delta, new_m, new_v = {}, {}, {}, {}

    def update(n, g2, shape2):
        d2, m2, v2 = _adamw(w[n].reshape(shape2), g2, m[n].reshape(shape2), v[n].reshape(shape2),
                            "adamw_" + n)
        g[n], delta[n], new_m[n], new_v[n] = (a.reshape(w[n].shape) for a in (g2, d2, m2, v2))

    small_params = []
    o = 0
    for n, shape2 in REPLICATED:
        if n in GATE_BLOCKS:
            k, rows = GATE_BLOCKS.index(n), gate_sum.shape[0] // len(GATE_BLOCKS)
            update(n, gate_sum[k * rows:(k + 1) * rows].reshape(shape2), shape2)
        else:
            size = shape2[0] * shape2[1]
            small_params.append((n, small[o:o + size].reshape(shape2), shape2))
            o += size
    chan_back = summed["chan"].reshape(-1)[:N_CHAN_ROWS * LANES].reshape(N_CHAN_ROWS, LANES)
    r0 = 0
    for n, rows in CHAN:
        small_params.append((n, chan_back[r0:r0 + rows], (rows, LANES)))
        r0 += rows
    results = _adamw_small([(w[n].reshape(s2), g2, m[n].reshape(s2), v[n].reshape(s2))
                            for n, g2, s2 in small_params], "adamw_vectors")
    for (n, g2, _), (d2, m2, v2) in zip(small_params, results):
        g[n], delta[n], new_m[n], new_v[n] = (a.reshape(w[n].shape) for a in (g2, d2, m2, v2))

    for n in ROW_SHARDED:
        update(n, summed[n], summed[n].shape)
    for n, t in TRANSPOSED.items():
        if t in summed:
            update(n, summed[t].T, summed[t].shape[::-1])
    summed = finish("late", late_sections, _after_all(list(delta.values()), "updates_done"), "late")
    g_t = summed["w_in_t"]
    results = _adamw(w["w_in"][0].T, g_t, m["w_in"][0].T, v["w_in"][0].T, "adamw_w_in")
    g["w_in"], delta["w_in"], new_m["w_in"], new_v["w_in"] = (a.T[None] for a in (g_t, *results))

    return (loss, grad_x[None], *[g[n] for n in NAMES], *[delta[n] for n in NAMES],
            *[new_m[n] for n in NAMES], *[new_v[n] for n in NAMES])
```

```python
import math

import numpy as np
import jax
import jax.numpy as jnp
from jax import lax
from jax.experimental import pallas as pl
from jax.experimental.pallas import tpu as pltpu

F32 = jnp.float32
BF16 = jnp.bfloat16

T = 2048
D = 1024
D_ATT = 512
D_REC = 1024
D_FF = 4096
D_IN = 5632
N_HEADS = 8
DH = 64
GRID_W = 64
ROWS = T // GRID_W
WIN_H = 8
WIN_W = 16
KWIN = WIN_H * GRID_W
N_RPB_R = 2 * WIN_H - 1
N_RPB_C = 2 * WIN_W - 1
N_REC_BLOCKS = 16
REC_BLOCK = 64
CG = 128
N_CG = D_REC // CG
LRU_C = 8.0
EPS = 1e-6
N_DEV = 8
N_CHIPS = 4
LANES = 128

ADAM_LR = 0.001
ADAM_B1 = 0.9
ADAM_B2 = 0.999
ADAM_EPS = 1e-08
ADAM_WD = 0.01
ADAM_STEP = 10

MESH_AXES = ("x", "y", "c")
VMEM_LIMIT = 56 * 1024 * 1024

TILE = 512
DZ_ARRAYS = ((0, 3, 1), (3, 4, 2), (7, 4, 2))
N_DZ_TILES = D_IN // TILE


def _params(**kw):
    return pltpu.CompilerParams(vmem_limit_bytes=VMEM_LIMIT, **kw)


HG = 4
HQ = HG * GRID_W
HC = HG * DH


def _att_tables():
    rq = np.arange(GRID_W)
    kc = np.arange(KWIN) % GRID_W
    win_start = np.clip(rq - WIN_W // 2, 0, GRID_W - WIN_W)
    valid = (kc[None, :] >= win_start[:, None]) & (kc[None, :] < win_start[:, None] + WIN_W)
    same_head = (np.arange(HQ)[:, None] // GRID_W) == (np.arange(HC)[None, :] // DH)
    return valid.astype(np.float32), same_head.astype(np.float32)


def _pair_mask():
    half = np.arange(2 * DH) // DH
    return (half[:, None] == half[None, :]).astype(np.float32)


def _dup_table():
    return np.concatenate([np.eye(REC_BLOCK, dtype=np.float32)] * 2, axis=1)


def _sigmoid(x):
    return 0.5 * jnp.tanh(0.5 * x) + 0.5


def _softplus(x):
    return jnp.maximum(x, 0.0) + jnp.log(1.0 + jnp.exp(-jnp.abs(x)))


def _one_minus_square(log_a, a):
    x = 2.0 * log_a
    series = -x * (1.0 + x * (0.5 + x * (1.0 / 6.0)))
    return jnp.where(x > -0.02, series, 1.0 - a * a)


_GELU_C = math.sqrt(2.0 / math.pi)


def _gelu_and_grad(x):
    x2 = x * x
    inner = _GELU_C * (x + 0.044715 * x * x2)
    t = jnp.tanh(inner)
    g = 0.5 * x * (1.0 + t)
    dg = 0.5 * (1.0 + t) + 0.5 * x * (1.0 - t * t) * _GELU_C * (1.0 + 3.0 * 0.044715 * x2)
    return g, dg


def _dot(a, b):
    return jnp.dot(a, b, preferred_element_type=F32)


def _dot_nt(a, b):
    return lax.dot_general(a, b, (((1,), (1,)), ((), ())), preferred_element_type=F32)


def _dot_tn(a, b):
    return lax.dot_general(a, b, (((0,), (0,)), ((), ())), preferred_element_type=F32)


def _dot_exact(a, b):
    return jnp.dot(a, b, precision=lax.Precision.HIGHEST, preferred_element_type=F32)


def _shift_rows(x, s):
    n = x.shape[0]
    rows = lax.broadcasted_iota(jnp.int32, x.shape, 0)
    y = pltpu.roll(x, s % n, 0)
    if s > 0:
        return jnp.where(rows >= s, y, 0.0)
    return jnp.where(rows < n + s, y, 0.0)


def _rms_bwd(dh, xh, r, g):
    dxh = dh * g
    return r * (dxh - xh * jnp.mean(dxh * xh, axis=-1, keepdims=True))


def _matmul(a, b, mode, out_dtype, name, tm=512, tn=1024, tk=2048):
    if mode == "nn":
        (m, k), (k2, n) = a.shape, b.shape
    elif mode == "nt":
        (m, k), (n, k2) = a.shape, b.shape
    else:
        (k, m), (k2, n) = a.shape, b.shape
    assert k == k2
    tm, tn, tk = min(tm, m), min(tn, n), min(tk, k)
    assert m % tm == 0 and n % tn == 0 and k % tk == 0
    nk = k // tk
    dot = {"nn": _dot, "nt": _dot_nt, "tn": _dot_tn}[mode]

    def body(a_ref, b_ref, o_ref, acc):
        kk = pl.program_id(2)
        part = dot(a_ref[...].astype(BF16), b_ref[...].astype(BF16))
        if nk == 1:
            o_ref[...] = part.astype(out_dtype)
            return

        @pl.when(kk == 0)
        def _():
            acc[...] = part

        @pl.when(kk > 0)
        def _():
            acc[...] += part

        @pl.when(kk == nk - 1)
        def _():
            o_ref[...] = acc[...].astype(out_dtype)

    if mode == "tn":
        a_spec = pl.BlockSpec((tk, tm), lambda i, j, kk: (kk, i))
    else:
        a_spec = pl.BlockSpec((tm, tk), lambda i, j, kk: (i, kk))
    if mode == "nt":
        b_spec = pl.BlockSpec((tn, tk), lambda i, j, kk: (j, kk))
    else:
        b_spec = pl.BlockSpec((tk, tn), lambda i, j, kk: (kk, j))
    return pl.pallas_call(
        body, name=name,
        out_shape=jax.ShapeDtypeStruct((m, n), out_dtype),
        grid=(m // tm, n // tn, nk),
        in_specs=[a_spec, b_spec],
        out_specs=pl.BlockSpec((tm, tn), lambda i, j, kk: (i, j)),
        scratch_shapes=[pltpu.VMEM((tm, tn) if nk > 1 else (8, LANES), F32)],
        compiler_params=_params(dimension_semantics=("parallel", "parallel", "arbitrary")),
    )(a, b)


def _in_proj(x, g1, w_in_t, b_in):
    tm = 512

    def body(x_ref, g_ref, w_hbm, b_ref, qkv_ref, uy_ref, gg_ref, h_ref, w):
        @pl.when(pl.program_id(0) == 0)
        def _():
            pltpu.sync_copy(w_hbm, w)

        xv = x_ref[...]
        r = lax.rsqrt(jnp.mean(xv * xv, axis=-1, keepdims=True) + EPS)
        h = ((xv * r) * g_ref[...]).astype(BF16)
        h_ref[...] = h
        row0 = 0
        for ref in (qkv_ref, uy_ref, gg_ref):
            for c0 in range(0, ref.shape[1], TILE):
                z = _dot_nt(h, w[row0:row0 + TILE, :]) + b_ref[:, row0:row0 + TILE]
                ref[:, c0:c0 + TILE] = z.astype(ref.dtype)
                row0 += TILE

    tok = lambda width: pl.BlockSpec((tm, width), lambda i: (i, 0))
    return pl.pallas_call(
        body, name="in_proj",
        out_shape=(jax.ShapeDtypeStruct((T, 3 * D_ATT), BF16),
                   jax.ShapeDtypeStruct((T, 2 * D_REC), F32),
                   jax.ShapeDtypeStruct((T, 2 * D), F32),
                   jax.ShapeDtypeStruct((T, D), BF16)),
        grid=(T // tm,),
        in_specs=[tok(D), pl.BlockSpec((1, D), lambda i: (0, 0)), pl.BlockSpec(memory_space=pl.ANY),
                  pl.BlockSpec((1, D_IN), lambda i: (0, 0))],
        out_specs=(tok(3 * D_ATT), tok(2 * D_REC), tok(2 * D), tok(D)),
        scratch_shapes=[pltpu.VMEM((D_IN, D), BF16)],
        compiler_params=_params(dimension_semantics=("arbitrary",)),
    )(x, g1, w_in_t, b_in)


def _dz_specs(rows, tile_of, row_of):
    def spec(off, n, per_plane):
        def index(*ids):
            t = jnp.clip(tile_of(*ids) - off, 0, n - 1)
            return (t // per_plane, row_of(*ids), t % per_plane)
        return pl.BlockSpec((1, rows, TILE), index)
    return [spec(off, n, per) for off, n, per in DZ_ARRAYS]


def _dh_norm1_bwd(dz, w_in_t, x, g1, dx1, after):
    tm = 512

    def body(dqkv_ref, duy_ref, dgg_ref, w_hbm, x_ref, g_ref, dx1_ref, after_ref, gx_ref, dg_ref, w):
        @pl.when(pl.program_id(0) == 0)
        def _():
            pltpu.sync_copy(w_hbm, w)
            dg_ref[...] = jnp.zeros_like(dg_ref)

        dh, row0 = None, 0
        for ref in (dqkv_ref, duy_ref, dgg_ref):
            for plane in range(ref.shape[0]):
                cols = ref.shape[2]
                part = _dot(ref[plane], w[row0:row0 + cols, :])
                dh = part if dh is None else dh + part
                row0 += cols
        xv = x_ref[...]
        r = lax.rsqrt(jnp.mean(xv * xv, axis=-1, keepdims=True) + EPS)
        xh = xv * r
        dg_ref[...] += jnp.sum(dh * xh, axis=0, keepdims=True)
        gx_ref[...] = dx1_ref[...] + _rms_bwd(dh, xh, r, g_ref[...])

    tok = pl.BlockSpec((tm, D), lambda i: (i, 0))
    vec = pl.BlockSpec((1, D), lambda i: (0, 0))
    planes = lambda a: pl.BlockSpec((a.shape[0], tm, a.shape[2]), lambda i: (0, i, 0))
    return pl.pallas_call(
        body, name="dh_norm1_bwd",
        out_shape=(jax.ShapeDtypeStruct((T, D), F32), jax.ShapeDtypeStruct((1, D), F32)),
        grid=(T // tm,),
        in_specs=[planes(a) for a in dz] + [pl.BlockSpec(memory_space=pl.ANY), tok, vec, tok,
                                            pl.BlockSpec(memory_space=pl.ANY)],
        out_specs=(tok, vec),
        scratch_shapes=[pltpu.VMEM((D_IN, D), BF16)],
        compiler_params=_params(dimension_semantics=("arbitrary",)),
    )(*dz, w_in_t, x, g1, dx1, after)


def _grad_w_in(dz, h):
    def body(*refs):
        seg_refs = refs[:3]
        h_ref, gw_ref, gb_ref = refs[3:]
        j = pl.program_id(0)

        for s, (off, n, _) in enumerate(DZ_ARRAYS):
            @pl.when((j >= off) & (j < off + n))
            def _(s=s):
                a = seg_refs[s][0]
                gw_ref[...] = _dot_tn(a, h_ref[...]).astype(BF16)
                gb_ref[...] = jnp.sum(a.astype(F32), axis=0, keepdims=True)

    return pl.pallas_call(
        body, name="grad_w_in",
        out_shape=(jax.ShapeDtypeStruct((D_IN, D), BF16), jax.ShapeDtypeStruct((1, D_IN), F32)),
        grid=(N_DZ_TILES,),
        in_specs=_dz_specs(T, lambda j: j, lambda j: 0) + [pl.BlockSpec((T, D), lambda j: (0, 0))],
        out_specs=(pl.BlockSpec((TILE, D), lambda j: (j, 0)), pl.BlockSpec((1, TILE), lambda j: (0, j))),
        compiler_params=_params(dimension_semantics=("parallel",)),
    )(*dz, h)


def _rpb_rows(rpb):
    padded = jnp.pad(rpb, ((0, 0), (0, 0), (0, GRID_W - N_RPB_C)))
    rows = [padded[:, WIN_H - 1 - oi: 2 * WIN_H - 1 - oi].reshape(N_HEADS // HG, HG, KWIN)
            for oi in range(WIN_H)]
    return jnp.stack(rows, axis=0)


SKEW = KWIN - (WIN_W - 1)


MASKED = -1e30


def _bias_tiles(rows_ref, valid, bias_s):
    for oi in range(WIN_H):
        for hh in range(HG):
            row = jnp.broadcast_to(rows_ref[oi, 0, hh:hh + 1, :], (GRID_W, KWIN))
            tile = pltpu.roll(row, SKEW, 1, stride=1, stride_axis=0)
            bias_s[oi, hh * GRID_W:(hh + 1) * GRID_W, :] = jnp.where(valid, tile, MASKED)


def _bias_tile_grads(gb_s, flip, out_ref):
    for oi in range(WIN_H):
        for hh in range(HG):
            g = _dot_exact(flip, gb_s[oi, hh * GRID_W:(hh + 1) * GRID_W, :])
            back = pltpu.roll(g, KWIN - (GRID_W - WIN_W), 1, stride=1, stride_axis=0)
            out_ref[0, oi, hh:hh + 1, :] = jnp.sum(back, axis=0, keepdims=True)


def _rpb_fold(row_grads):
    g = row_grads.transpose(1, 0, 2, 3).reshape(WIN_H, N_HEADS, WIN_H, GRID_W)
    g = g.transpose(0, 2, 1, 3)

    def body(g_ref, o_ref):
        for dr in range(N_RPB_R):
            terms = [g_ref[oi, i] for oi in range(WIN_H) for i in range(WIN_H) if i - oi + WIN_H - 1 == dr]
            acc = terms[0]
            for term in terms[1:]:
                acc = acc + term
            o_ref[dr] = acc

    out = pl.pallas_call(
        body, name="rpb_fold",
        out_shape=jax.ShapeDtypeStruct((N_RPB_R, N_HEADS, GRID_W), F32),
    )(g)
    return out.transpose(1, 0, 2)[:, :, :N_RPB_C]


ATT_GROUPS = N_HEADS // HG
ATT_UNROLL = 8


def _stacked(rows64, same_head):
    return jnp.where(same_head, jnp.concatenate([rows64] * HG, axis=0), jnp.zeros((), BF16))


def _own_heads(stacked):
    head = lax.broadcasted_iota(jnp.int32, (GRID_W, HC), 1) // DH
    out = stacked[:GRID_W]
    for h in range(1, HG):
        out = jnp.where(head == h, stacked[h * GRID_W:(h + 1) * GRID_W], out)
    return out


def _att_scores(q_ref, k_ref, bias_ref, same_head, r):
    rs = jnp.clip(r - WIN_H // 2, 0, ROWS - WIN_H)
    oi = r - rs
    q0 = pl.multiple_of(r * GRID_W, GRID_W)
    k0 = pl.multiple_of(rs * GRID_W, GRID_W)
    q2 = _stacked(q_ref[pl.ds(q0, GRID_W), :] * (DH ** -0.5), same_head)
    kw = k_ref[pl.ds(k0, KWIN), :]
    s = _dot_nt(q2, kw) + bias_ref[oi]
    e = jnp.exp(s - jnp.max(s, axis=-1, keepdims=True))
    return e, 1.0 / jnp.sum(e, axis=-1, keepdims=True), q2, kw, q0, k0, oi


def _att_specs():
    col = lambda off: pl.BlockSpec((T, HC), lambda g: (0, g + off * ATT_GROUPS))
    tables = [pl.BlockSpec((WIN_H, 1, HG, KWIN), lambda g: (0, g, 0, 0)),
              pl.BlockSpec((GRID_W, KWIN), lambda g: (0, 0)),
              pl.BlockSpec((HQ, HC), lambda g: (0, 0))]
    return col, tables, pltpu.VMEM((WIN_H, HQ, KWIN), F32)


def _att_fwd(qkv, bias_rows):
    valid_np, same_head_np = _att_tables()

    def body(q_ref, k_ref, v_ref, rows_ref, valid_ref, head_ref, o_ref, bias_s):
        same_head = head_ref[...] > 0.5
        _bias_tiles(rows_ref, valid_ref[...] > 0.5, bias_s)

        def row(r, carry):
            e, rl, _, _, q0, k0, _ = _att_scores(q_ref, k_ref, bias_s, same_head, r)
            o2 = _dot((e * rl).astype(BF16), v_ref[pl.ds(k0, KWIN), :])
            o_ref[pl.ds(q0, GRID_W), :] = _own_heads(o2).astype(BF16)
            return carry

        lax.fori_loop(0, ROWS, row, 0, unroll=ATT_UNROLL)

    col, tables, tiles = _att_specs()
    return pl.pallas_call(
        body, name="att_fwd",
        out_shape=jax.ShapeDtypeStruct((T, D_ATT), BF16),
        grid=(ATT_GROUPS,),
        in_specs=[col(0), col(1), col(2)] + tables,
        out_specs=col(0),
        scratch_shapes=[tiles],
        compiler_params=_params(dimension_semantics=("parallel",)),
    )(qkv, qkv, qkv, bias_rows, jnp.asarray(valid_np), jnp.asarray(same_head_np))


def _att_bwd(qkv, bias_rows, datt, after):
    valid_np, same_head_np = _att_tables()

    def body(q_ref, k_ref, v_ref, do_ref, rows_ref, valid_ref, head_ref, flip_ref,
             dqkv_ref, grows_ref, dk_acc, dv_acc, bias_s, gb_s):
        same_head = head_ref[...] > 0.5
        dk_acc[...] = jnp.zeros_like(dk_acc)
        dv_acc[...] = jnp.zeros_like(dv_acc)
        gb_s[...] = jnp.zeros_like(gb_s)
        _bias_tiles(rows_ref, valid_ref[...] > 0.5, bias_s)

        def row(r, carry):
            e, rl, q2, kw, q0, k0, oi = _att_scores(q_ref, k_ref, bias_s, same_head, r)
            do2 = _stacked(do_ref[pl.ds(q0, GRID_W), :], same_head)
            vw = v_ref[pl.ds(k0, KWIN), :]
            p = e * rl
            dp = _dot_nt(do2, vw)
            ds = p * (dp - jnp.sum(dp * p, axis=-1, keepdims=True))
            p16 = p.astype(BF16)
            ds16 = ds.astype(BF16)
            dv_acc[pl.ds(k0, KWIN), :] += _dot_tn(p16, do2)
            dk_acc[pl.ds(k0, KWIN), :] += _dot_tn(ds16, q2)
            dq2 = _dot(ds16, kw) * (DH ** -0.5)
            dqkv_ref[0, pl.ds(q0, GRID_W), :] = _own_heads(dq2).astype(BF16)
            gb_s[oi] += ds
            return carry

        lax.fori_loop(0, ROWS, row, 0, unroll=ATT_UNROLL)
        dqkv_ref[1] = dk_acc[...].astype(BF16)
        dqkv_ref[2] = dv_acc[...].astype(BF16)
        _bias_tile_grads(gb_s, flip_ref[...], grows_ref)

    col, tables, tiles = _att_specs()
    return pl.pallas_call(
        body, name="att_bwd",
        out_shape=(jax.ShapeDtypeStruct((3, T, D_ATT), BF16),
                   jax.ShapeDtypeStruct((ATT_GROUPS, WIN_H, HG, KWIN), F32)),
        grid=(ATT_GROUPS,),
        in_specs=[col(0), col(1), col(2), col(0)] + tables + [pl.BlockSpec((GRID_W, GRID_W), lambda g: (0, 0))],
        out_specs=(pl.BlockSpec((3, T, HC), lambda g: (0, 0, g)),
                   pl.BlockSpec((1, WIN_H, HG, KWIN), lambda g: (g, 0, 0, 0))),
        scratch_shapes=[pltpu.VMEM((T, HC), F32), pltpu.VMEM((T, HC), F32), tiles, tiles],
        compiler_params=_params(dimension_semantics=("parallel",)),
    )(qkv, qkv, qkv, datt, bias_rows, jnp.asarray(valid_np) + after, jnp.asarray(same_head_np),
      jnp.asarray(np.eye(GRID_W, dtype=np.float32)[::-1].copy()))


def _conv_taps(up):
    return (_shift_rows(up, 2), _shift_rows(up, 1), up, _shift_rows(up, -1))


def _pair_block_diag(w_pair, dup, same_half):
    return jnp.where(same_half, _dot(w_pair.astype(BF16), dup), 0.0).astype(BF16)


def _gates(u, u16, wa, ba, wi, bi, lam):
    r = _sigmoid(_dot(u16, wa) + ba)
    ig = _sigmoid(_dot(u16, wi) + bi)
    sp = _softplus(-lam)
    log_a = (-LRU_C) * r * sp
    a = jnp.exp(log_a)
    mult2 = jnp.maximum(_one_minus_square(log_a, a), 0.0)
    return r, ig, sp, a, jnp.sqrt(mult2), mult2


SCAN_BLOCKS = 8


def _scans(jobs):
    c = jobs[0][0].shape[1]
    nblk = T // 8
    rows = lax.broadcasted_iota(jnp.int32, (8, c), 0)

    def block(a, b, reverse):
        for s in (1, 2, 4):
            if reverse:
                keep = rows < 8 - s
                a_s = jnp.where(keep, pltpu.roll(a, 8 - s, 0), 1.0)
                b_s = jnp.where(keep, pltpu.roll(b, 8 - s, 0), 0.0)
            else:
                keep = rows >= s
                a_s = jnp.where(keep, pltpu.roll(a, s, 0), 1.0)
                b_s = jnp.where(keep, pltpu.roll(b, s, 0), 0.0)
            b = a * b_s + b
            a = a * a_s
        return a, b

    def step(i, carry):
        out = []
        for (a_ref, b_ref, h_ref, reverse), h_prev in zip(jobs, carry):
            for u in range(SCAN_BLOCKS):
                blk = i * SCAN_BLOCKS + u
                if reverse:
                    blk = nblk - 1 - blk
                t0 = pl.multiple_of(blk * 8, 8)
                a, b = block(a_ref[pl.ds(t0, 8), :], b_ref[pl.ds(t0, 8), :], reverse)
                h = a * h_prev + b
                h_ref[pl.ds(t0, 8), :] = h
                h_prev = jnp.broadcast_to(h[0:1] if reverse else h[7:8], (8, c))
            out.append(h_prev)
        return tuple(out)

    lax.fori_loop(0, nblk // SCAN_BLOCKS, step, tuple(jnp.zeros((8, c), F32) for _ in jobs))


def _rec_specs():
    tok = lambda off: pl.BlockSpec((T, CG), lambda g: (0, g + off))
    per_ch = lambda rows: pl.BlockSpec((rows, CG), lambda g: (0, g))
    wspec = pl.BlockSpec((2, 1, CG, REC_BLOCK), lambda g: (0, g, 0, 0))
    const = lambda shape: pl.BlockSpec(shape, lambda g: (0, 0))
    return tok, per_ch, wspec, const


def _rec_fwd(uy, conv_w, conv_b, w_a, b_a, w_i, b_i, lam):
    tok, per_ch, wspec, const = _rec_specs()

    def body(up_ref, yb_ref, cw_ref, cb_ref, wa_ref, ba_ref, wi_ref, bi_ref, lam_ref, dup_ref, half_ref,
             hf_ref, hb_ref, yrec_ref, am_ref, bx_f, bx_b):
        dup = dup_ref[...]
        same_half = half_ref[...] > 0.5
        taps = _conv_taps(up_ref[...])
        u = cb_ref[...]
        for j in range(4):
            u = u + taps[j] * cw_ref[j:j + 1, :]
        u16 = u.astype(BF16)
        for d, bx_s in enumerate((bx_f, bx_b)):
            wa = _pair_block_diag(wa_ref[d, 0], dup, same_half)
            wi = _pair_block_diag(wi_ref[d, 0], dup, same_half)
            _, ig, _, a, mult, _ = _gates(u, u16, wa, ba_ref[d:d + 1, :], wi, bi_ref[d:d + 1, :],
                                       lam_ref[d:d + 1, :])
            am_ref[2 * d] = a
            am_ref[2 * d + 1] = mult
            bx_s[...] = mult * (ig * u)
        _scans([(am_ref.at[0], bx_f, hf_ref, False), (am_ref.at[2], bx_b, hb_ref, True)])
        gelu, _ = _gelu_and_grad(yb_ref[...])
        yrec_ref[...] = ((hf_ref[...] + hb_ref[...]) * gelu).astype(BF16)

    return pl.pallas_call(
        body, name="rec_fwd",
        out_shape=(jax.ShapeDtypeStruct((T, D_REC), F32), jax.ShapeDtypeStruct((T, D_REC), F32),
                   jax.ShapeDtypeStruct((T, D_REC), BF16), jax.ShapeDtypeStruct((4, T, D_REC), F32)),
        grid=(N_CG,),
        in_specs=[tok(0), tok(N_CG), per_ch(4), per_ch(1), wspec, per_ch(2), wspec, per_ch(2), per_ch(2),
                  const((REC_BLOCK, CG)), const((CG, CG))],
        out_specs=(tok(0), tok(0), tok(0), pl.BlockSpec((4, T, CG), lambda g: (0, 0, g))),
        scratch_shapes=[pltpu.VMEM((T, CG), F32)] * 2,
        compiler_params=_params(dimension_semantics=("parallel",)),
    )(uy, uy, conv_w, conv_b, w_a, b_a, w_i, b_i, lam,
      jnp.asarray(_dup_table(), BF16), jnp.asarray(_pair_mask()))


def _rec_bwd(uy, hf, hb, am, dyrec, conv_w, conv_b, w_a, b_a, w_i, b_i, lam):
    tok, per_ch, wspec, const = _rec_specs()

    def body(up_ref, yb_ref, hf_ref, hb_ref, am_ref, dy_ref, cw_ref, cb_ref, wa_ref, ba_ref, wi_ref, bi_ref,
             lam_ref, dup_ref, dupt_ref, half_ref,
             duy_ref, dcw_ref, dcb_ref, dwa_ref, dba_ref, dwi_ref, dbi_ref, dlam_ref,
             a_s0, a_s1, dh_s, g_s0, g_s1):
        dup = dup_ref[...]
        dup_t = dupt_ref[...]
        same_half = half_ref[...] > 0.5
        taps = _conv_taps(up_ref[...])
        u = cb_ref[...]
        for j in range(4):
            u = u + taps[j] * cw_ref[j:j + 1, :]
        u16 = u.astype(BF16)
        gelu, dgelu = _gelu_and_grad(yb_ref[...])
        dy = dy_ref[...]
        duy_ref[1] = (dy * (hf_ref[...] + hb_ref[...]) * dgelu).astype(BF16)
        dh_s[...] = dy * gelu
        a_s0[...] = _shift_rows(am_ref[0], -1)
        a_s1[...] = _shift_rows(am_ref[2], 1)
        _scans([(a_s0, dh_s, g_s0, True), (a_s1, dh_s, g_s1, False)])
        du = jnp.zeros((T, CG), F32)
        for d, g_s in enumerate((g_s0, g_s1)):
            reverse = d == 1
            wa = _pair_block_diag(wa_ref[d, 0], dup, same_half)
            wi = _pair_block_diag(wi_ref[d, 0], dup, same_half)
            lam_d = lam_ref[d:d + 1, :]
            r = _sigmoid(_dot(u16, wa) + ba_ref[d:d + 1, :])
            ig = _sigmoid(_dot(u16, wi) + bi_ref[d:d + 1, :])
            sp = _softplus(-lam_d)
            a, mult = am_ref[2 * d], am_ref[2 * d + 1]
            mult2 = mult * mult
            g = g_s[...]
            h_prev = _shift_rows(hb_ref[...], -1) if reverse else _shift_rows(hf_ref[...], 1)
            da = g * h_prev
            dmult = g * (ig * u)
            dig = g * mult * u
            du = du + g * mult * ig
            dmult_dlog = jnp.where(mult2 > 0.0, -(a * a) * lax.rsqrt(mult2), 0.0)
            dlog_a = da * a + dmult * dmult_dlog
            dr = dlog_a * ((-LRU_C) * sp)
            dsp = jnp.sum(dlog_a * ((-LRU_C) * r), axis=0, keepdims=True)
            dlam_ref[d:d + 1, :] = dsp * (-_sigmoid(-lam_d))
            dga = dr * r * (1.0 - r)
            dgi = dig * ig * (1.0 - ig)
            dga16 = dga.astype(BF16)
            dgi16 = dgi.astype(BF16)
            du = du + _dot_nt(dga16, wa) + _dot_nt(dgi16, wi)
            dwa_ref[d, 0] = _dot_exact(jnp.where(same_half, _dot_tn(u16, dga16), 0.0), dup_t)
            dwi_ref[d, 0] = _dot_exact(jnp.where(same_half, _dot_tn(u16, dgi16), 0.0), dup_t)
            dba_ref[d:d + 1, :] = jnp.sum(dga, axis=0, keepdims=True)
            dbi_ref[d:d + 1, :] = jnp.sum(dgi, axis=0, keepdims=True)
        dcb_ref[...] = jnp.sum(du, axis=0, keepdims=True)
        for j in range(4):
            dcw_ref[j:j + 1, :] = jnp.sum(du * taps[j], axis=0, keepdims=True)
        dup_in = (_shift_rows(du, -2) * cw_ref[0:1, :] + _shift_rows(du, -1) * cw_ref[1:2, :]
                  + du * cw_ref[2:3, :] + _shift_rows(du, 1) * cw_ref[3:4, :])
        duy_ref[0] = dup_in.astype(BF16)

    wshape = jax.ShapeDtypeStruct((2, N_CG, CG, REC_BLOCK), F32)
    vec = lambda rows: jax.ShapeDtypeStruct((rows, D_REC), F32)
    dup_np = _dup_table()
    return pl.pallas_call(
        body, name="rec_bwd",
        out_shape=(jax.ShapeDtypeStruct((2, T, D_REC), BF16),
                   vec(4), vec(1), wshape, vec(2), wshape, vec(2), vec(2)),
        grid=(N_CG,),
        in_specs=[tok(0), tok(N_CG), tok(0), tok(0), pl.BlockSpec((4, T, CG), lambda g: (0, 0, g)), tok(0),
                  per_ch(4), per_ch(1), wspec, per_ch(2), wspec, per_ch(2), per_ch(2),
                  const((REC_BLOCK, CG)), const((CG, REC_BLOCK)), const((CG, CG))],
        out_specs=(pl.BlockSpec((2, T, CG), lambda g: (0, 0, g)),
                   per_ch(4), per_ch(1), wspec, per_ch(2), wspec, per_ch(2), per_ch(2)),
        scratch_shapes=[pltpu.VMEM((T, CG), F32)] * 5,
        compiler_params=_params(dimension_semantics=("parallel",)),
    )(uy, uy, hf, hb, am, dyrec, conv_w, conv_b, w_a, b_a, w_i, b_i, lam,
      jnp.asarray(dup_np, BF16), jnp.asarray(dup_np.T.copy()), jnp.asarray(_pair_mask()))


TM_MIX = 256


def _mix_specs():
    tok = lambda width, blk=0: pl.BlockSpec((TM_MIX, width), lambda i: (i, blk))
    full = lambda shape: pl.BlockSpec(shape, lambda i: (0, 0))
    return tok, full


def _mix_fwd(x, att, yrec, gg, w_att_o_t, w_rec_o, w_out):
    tok, full = _mix_specs()

    def body(x_ref, att_ref, yr_ref, ga_ref, gr_ref, wao_ref, wro_ref, wo_ref, x1_ref, mixed_ref):
        y_att = _dot_nt(att_ref[...], wao_ref[...])
        y_rec = _dot(yr_ref[...], wro_ref[...])
        mixed = (_sigmoid(ga_ref[...]) * y_att + _sigmoid(gr_ref[...]) * y_rec).astype(BF16)
        mixed_ref[...] = mixed
        x1_ref[...] = x_ref[...] + _dot(mixed, wo_ref[...])

    return pl.pallas_call(
        body, name="mix_fwd",
        out_shape=(jax.ShapeDtypeStruct((T, D), F32), jax.ShapeDtypeStruct((T, D), BF16)),
        grid=(T // TM_MIX,),
        in_specs=[tok(D), tok(D_ATT), tok(D_REC), tok(D, 0), tok(D, 1),
                  full((D, D_ATT)), full((D_REC, D)), full((D, D))],
        out_specs=(tok(D), tok(D)),
        compiler_params=_params(dimension_semantics=("parallel",)),
    )(x, att, yrec, gg, gg, w_att_o_t, w_rec_o, w_out)


def _mix_bwd(dx1, att, yrec, gg, w_att_o_t, w_rec_o, w_out, after):
    tok, full = _mix_specs()

    def body(dx_ref, att_ref, yr_ref, ga_ref, gr_ref, wao_ref, wro_ref, wo_ref, after_ref,
             dgg_ref, dya_ref, dyr_ref, datt_ref, dyrp_ref):
        dmixed = _dot_nt(dx_ref[...].astype(BF16), wo_ref[...])
        y_att = _dot_nt(att_ref[...], wao_ref[...])
        y_rec = _dot(yr_ref[...], wro_ref[...])
        sa = _sigmoid(ga_ref[...])
        sr = _sigmoid(gr_ref[...])
        dgg_ref[0] = (dmixed * y_att * sa * (1.0 - sa)).astype(BF16)
        dgg_ref[1] = (dmixed * y_rec * sr * (1.0 - sr)).astype(BF16)
        dya = (dmixed * sa).astype(BF16)
        dyr = (dmixed * sr).astype(BF16)
        dya_ref[...] = dya
        dyr_ref[...] = dyr
        datt_ref[...] = _dot(dya, wao_ref[...]).astype(BF16)
        dyrp_ref[...] = _dot_nt(dyr, wro_ref[...])

    return pl.pallas_call(
        body, name="mix_bwd",
        out_shape=(jax.ShapeDtypeStruct((2, T, D), BF16),
                   jax.ShapeDtypeStruct((T, D), BF16), jax.ShapeDtypeStruct((T, D), BF16),
                   jax.ShapeDtypeStruct((T, D_ATT), BF16), jax.ShapeDtypeStruct((T, D_REC), F32)),
        grid=(T // TM_MIX,),
        in_specs=[tok(D), tok(D_ATT), tok(D_REC), tok(D, 0), tok(D, 1),
                  full((D, D_ATT)), full((D_REC, D)), full((D, D)), pl.BlockSpec(memory_space=pl.ANY)],
        out_specs=(pl.BlockSpec((2, TM_MIX, D), lambda i: (0, i, 0)),
                   tok(D), tok(D), tok(D_ATT), tok(D_REC)),
        compiler_params=_params(dimension_semantics=("parallel",)),
    )(dx1, att, yrec, gg, gg, w_att_o_t, w_rec_o, w_out, after)


TM_FFN = 256
FF_CHUNK = 1024


def _ffn_loss(x1, target, g2, gf, w_ff1_t, w_ff2):
    n_chunks = D_FF // FF_CHUNK

    def body(x1_ref, tg_ref, g2_ref, gf_ref, w1_hbm, w2_hbm,
             loss_ref, dx1_ref, h2_ref, act_ref, dpre_ref, dx2_ref, dg2_ref, dgf_ref,
             w1, w2, relu_s):
        i = pl.program_id(0)

        @pl.when(i == 0)
        def _():
            pltpu.sync_copy(w1_hbm, w1)
            pltpu.sync_copy(w2_hbm, w2)
            loss_ref[...] = jnp.zeros_like(loss_ref)
            dg2_ref[...] = jnp.zeros_like(dg2_ref)
            dgf_ref[...] = jnp.zeros_like(dgf_ref)

        x1v = x1_ref[...]
        r2 = lax.rsqrt(jnp.mean(x1v * x1v, axis=-1, keepdims=True) + EPS)
        xh2 = x1v * r2
        h2 = (xh2 * g2_ref[...]).astype(BF16)
        h2_ref[...] = h2
        x2 = x1v
        for c in range(n_chunks):
            ff = slice(c * FF_CHUNK, (c + 1) * FF_CHUNK)
            rl = jnp.maximum(_dot_nt(h2, w1[ff, :]), 0.0)
            relu_s[:, ff] = rl
            act = (rl * rl).astype(BF16)
            act_ref[:, ff] = act
            x2 = x2 + _dot(act, w2[ff, :])
        r3 = lax.rsqrt(jnp.mean(x2 * x2, axis=-1, keepdims=True) + EPS)
        xh3 = x2 * r3
        err = xh3 * gf_ref[...] - tg_ref[...]
        loss_ref[...] += 0.5 * jnp.sum(jnp.mean(err * err, axis=-1, keepdims=True))
        dy = err * (1.0 / D)
        dgf_ref[...] += jnp.sum(dy * xh3, axis=0, keepdims=True)
        dx2 = _rms_bwd(dy, xh3, r3, gf_ref[...])
        dx2_16 = dx2.astype(BF16)
        dx2_ref[...] = dx2_16
        dh2 = jnp.zeros((TM_FFN, D), F32)
        for c in range(n_chunks):
            ff = slice(c * FF_CHUNK, (c + 1) * FF_CHUNK)
            dpre = (_dot_nt(dx2_16, w2[ff, :]) * (2.0 * relu_s[:, ff])).astype(BF16)
            dpre_ref[:, ff] = dpre
            dh2 = dh2 + _dot(dpre, w1[ff, :])
        dg2_ref[...] += jnp.sum(dh2 * xh2, axis=0, keepdims=True)
        dx1_ref[...] = dx2 + _rms_bwd(dh2, xh2, r2, g2_ref[...])

    tok = lambda width: pl.BlockSpec((TM_FFN, width), lambda i: (i, 0))
    vec = pl.BlockSpec((1, D), lambda i: (0, 0))
    hbm = pl.BlockSpec(memory_space=pl.ANY)
    return pl.pallas_call(
        body, name="ffn_loss",
        out_shape=(jax.ShapeDtypeStruct((8, 128), F32), jax.ShapeDtypeStruct((T, D), F32),
                   jax.ShapeDtypeStruct((T, D), BF16), jax.ShapeDtypeStruct((T, D_FF), BF16),
                   jax.ShapeDtypeStruct((T, D_FF), BF16), jax.ShapeDtypeStruct((T, D), BF16),
                   jax.ShapeDtypeStruct((1, D), F32), jax.ShapeDtypeStruct((1, D), F32)),
        grid=(T // TM_FFN,),
        in_specs=[tok(D), tok(D), vec, vec, hbm, hbm],
        out_specs=(pl.BlockSpec((8, 128), lambda i: (0, 0)), tok(D), tok(D), tok(D_FF), tok(D_FF), tok(D),
                   vec, vec),
        scratch_shapes=[pltpu.VMEM((D_FF, D), BF16), pltpu.VMEM((D_FF, D), BF16),
                        pltpu.VMEM((TM_FFN, D_FF), F32)],
        compiler_params=_params(dimension_semantics=("arbitrary",)),
    )(x1, target, g2, gf, w_ff1_t, w_ff2)


def _local_step(x, target, p, late_weights, reduce_first, reduce_early, reduce_late):
    bias = _rpb_rows(p["rpb"])
    pairs = lambda w: w.reshape(2, N_CG, CG, REC_BLOCK)
    w_a, w_i = pairs(p["w_rg_a"]), pairs(p["w_rg_i"])
    rec_params = (p["conv_w"], p["conv_b"], w_a, p["b_rg_a"], w_i, p["b_rg_i"], p["lru_lambda"])

    qkv, uy, gg, h = _in_proj(x, p["ln1_g"], p["w_in_t"], p["b_in"])
    att = _att_fwd(qkv, bias)
    hf, hb, yrec, am = _rec_fwd(uy, *rec_params)
    p = {**p, **late_weights(yrec, 0)}
    x1, mixed = _mix_fwd(x, att, yrec, gg, p["w_att_o_t"], p["w_rec_o"], p["w_out"])
    p = {**p, **late_weights(x1, 1)}
    loss8, dx1, h2, act, dpre, dx2, g_ln2, g_lnf = _ffn_loss(
        x1, target, p["ln2_g"], p["lnf_g"], p["w_ff1_t"], p["w_ff2"])

    grads = {"ln2_g": g_ln2, "lnf_g": g_lnf,
             "w_ff1_t": _matmul(dpre, h2, "tn", BF16, "g_w_ff1"),
             "w_ff2": _matmul(act, dx2, "tn", BF16, "g_w_ff2")}
    dgg, dya, dyr, datt, dyrp = _mix_bwd(dx1, att, yrec, gg, p["w_att_o_t"], p["w_rec_o"], p["w_out"],
                                         reduce_first(grads, None))
    lam_after = rec_params[-1] + reduce_first(None, dgg)[0, 0]
    duy, g_cw, g_cb, g_wa, g_ba, g_wi, g_bi, g_lam = _rec_bwd(uy, hf, hb, am, dyrp, *rec_params[:-1], lam_after)
    blocks = lambda g: g.reshape(2, N_REC_BLOCKS, REC_BLOCK, REC_BLOCK)
    grads.update({
        "w_att_o_t": _matmul(dya, att, "tn", BF16, "g_w_att_o"),
        "conv_w": g_cw, "conv_b": g_cb, "w_rg_a": blocks(g_wa), "b_rg_a": g_ba,
        "w_rg_i": blocks(g_wi), "b_rg_i": g_bi, "lru_lambda": g_lam,
        "w_rec_o": _matmul(yrec, dyr, "tn", BF16, "g_w_rec_o"),
        "w_out": _matmul(mixed, dx1, "tn", BF16, "g_w_out"),
    })
    dqkv, gbias = _att_bwd(qkv, bias, datt, reduce_early(grads))
    dz = (dqkv, duy, dgg)
    g_w_in_t, g_b_in = _grad_w_in(dz, h)
    grads.update(w_in_t=g_w_in_t, b_in=g_b_in)
    grad_x, g_ln1 = _dh_norm1_bwd(dz, p["w_in_t"], x, p["ln1_g"], dx1, reduce_late(grads))
    grads.update(ln1_g=g_ln1, rpb=_rpb_fold(gbias))
    return loss8[0:1, 0:1], grad_x, grads


MESH_ID = pl.DeviceIdType.MESH
ANY = pl.BlockSpec(memory_space=pl.ANY)

CHAN_BLOCK_ROWS = 32
GATE_ROWS = 2 * 2 * N_REC_BLOCKS * REC_BLOCK * REC_BLOCK // (N_DEV * D)
SECTIONS = (("w_in_t", 704, D), ("w_rec_o", 128, D), ("w_out", 128, D), ("w_ff1_t", 512, D),
            ("w_ff2", 512, D), ("chan", CHAN_BLOCK_ROWS, D), ("w_att_o_t", 128, D_ATT),
            ("gates", GATE_ROWS, D))
N_SEC = len(SECTIONS)
N_CHAN_ROWS = 10
CHAN = (("conv_w", 4), ("b_rg_a", 2), ("b_rg_i", 2), ("lru_lambda", 2))


def _position():
    return lax.axis_index("x"), lax.axis_index("y"), lax.axis_index("c")


def _other_chips(x, y):
    return [(1 - x, y), (x, 1 - y), (1 - x, 1 - y)]


PASS_ON_IDS, PAIR_EARLY_ID, PAIR_LATE_ID, PAIR_FIRST_ID = (1, 4), 2, 3, 5


def _pair_handshake(x, y, c):
    barrier = pltpu.get_barrier_semaphore()
    pl.semaphore_signal(barrier, inc=1, device_id=(x, y, 1 - c), device_id_type=MESH_ID)
    pl.semaphore_wait(barrier, 1)


def _block_of(ref, dev, rows):
    return ref.at[pl.ds(pl.multiple_of(dev * rows, 16), rows)]


def _all_gather(shards, name, relay=True):
    ns = len(shards)

    def body(*refs):
        x_refs, out_refs, done_ref = refs[:ns], refs[ns:2 * ns], refs[2 * ns]
        send_sems, recv_sems, local_sems = refs[2 * ns + 1:]
        done_ref[0, 0] = 0.0
        x, y, c = _position()
        me, sibling = (x, y, c), (x, y, 1 - c)
        x_nbr, y_nbr, diagonal = _other_chips(x, y)
        north = c == 1
        relay_from = (jnp.where(north, x_nbr[0], y_nbr[0]), jnp.where(north, x_nbr[1], y_nbr[1]))
        relay_to = (jnp.where(north, y_nbr[0], x_nbr[0]), jnp.where(north, y_nbr[1], x_nbr[1]))

        def rows(s, px, py, pc):
            return _block_of(out_refs[s], 4 * px + 2 * py + pc, shards[s].shape[0])

        def copy(k, s, block, to, from_shard=False):
            return pltpu.make_async_remote_copy(
                src_ref=x_refs[s] if from_shard else rows(s, *block), dst_ref=rows(s, *block),
                send_sem=send_sems.at[k * ns + s], recv_sem=recv_sems.at[k * ns + s],
                device_id=to, device_id_type=MESH_ID)

        sections = range(ns)
        mine = [pltpu.make_async_copy(x_refs[s], rows(s, *me), local_sems.at[s]) for s in sections]
        first_to = (sibling, (*x_nbr, c), (*y_nbr, c)) + (() if relay else ((*diagonal, c),))
        sent = [copy(k, s, me, to, True) for k, to in enumerate(first_to) for s in sections]
        for cp in mine + sent:
            cp.start()
        for s in sections:
            copy(1, s, (*x_nbr, c), me).wait_recv()
            copy(2, s, (*y_nbr, c), me).wait_recv()
            passed = [copy(3, s, (*relay_from, c), (*relay_to, c))] if relay else []
            passed += [copy(4, s, (*x_nbr, c), sibling), copy(5, s, (*y_nbr, c), sibling)]
            for cp in passed:
                cp.start()
            sent += passed
        for s in sections:
            copy(3, s, (*diagonal, c), me).wait_recv()
            sent.append(copy(6, s, (*diagonal, c), sibling))
            sent[-1].start()
        for s in sections:
            copy(0, s, sibling, me).wait_recv()
            for k, chip in ((4, x_nbr), (5, y_nbr), (6, diagonal)):
                copy(k, s, (*chip, 1 - c), me).wait_recv()
        for cp in sent:
            cp.wait_send()
        for cp in mine:
            cp.wait()

    return pl.pallas_call(
        body, name=name,
        out_shape=tuple(jax.ShapeDtypeStruct((N_DEV * s.shape[0], s.shape[1]), s.dtype) for s in shards)
        + (jax.ShapeDtypeStruct((1, 1), F32),),
        in_specs=[ANY] * ns,
        out_specs=(ANY,) * ns + (pl.BlockSpec(memory_space=pltpu.SMEM),),
        scratch_shapes=[pltpu.SemaphoreType.DMA((7 * ns,)), pltpu.SemaphoreType.DMA((7 * ns,)),
                        pltpu.SemaphoreType.DMA((ns,))],
    )(*shards)


HBM = pl.BlockSpec(memory_space=pltpu.HBM)
SEM = pl.BlockSpec(memory_space=pltpu.SEMAPHORE)
EFFECT = pltpu.SideEffectType.DATAFLOW_SIDE_EFFECTING


def _in_hbm(a):
    return pltpu.with_memory_space_constraint(a, pltpu.HBM)


def _first_hop_copies(shards, x_refs, zones, send_sems, recv_sems):
    ns = len(shards)
    x, y, c = _position()
    targets = [(x, y, 1 - c)] + [(cx, cy, c) for cx, cy in _other_chips(x, y)]
    return [pltpu.make_async_remote_copy(
        src_ref=x_refs[s], dst_ref=_block_of(zones[s], 4 * x + 2 * y + c, shards[s].shape[0]),
        send_sem=send_sems.at[k * ns + s], recv_sem=recv_sems.at[k * ns + s],
        device_id=to, device_id_type=MESH_ID)
        for k, to in enumerate(targets) for s in range(ns)]


def _after_all(arrays, name):
    def body(*refs):
        refs[-1][...] = jnp.zeros_like(refs[-1])

    return pl.pallas_call(
        body, name=name,
        out_shape=jax.ShapeDtypeStruct((8, LANES), F32),
        in_specs=[pl.BlockSpec(memory_space=pl.ANY)] * len(arrays),
        out_specs=pl.BlockSpec(memory_space=pltpu.VMEM),
    )(*arrays)


def _own_blocks_placed(shards, after):
    ns = len(shards)
    x, y, c = _position()
    me = jnp.reshape(4 * x + 2 * y + c, (1,)).astype(jnp.int32)
    shards = [*shards[:-1], shards[-1] + after.astype(shards[-1].dtype)]

    def body(me_ref, *refs):
        for s in range(ns):
            refs[ns + s][...] = refs[s][...]

    return pl.pallas_call(
        body, name="own_blocks_placed",
        out_shape=tuple(jax.ShapeDtypeStruct((N_DEV * s.shape[0], s.shape[1]), s.dtype) for s in shards),
        grid_spec=pltpu.PrefetchScalarGridSpec(
            num_scalar_prefetch=1, grid=(1,),
            in_specs=[pl.BlockSpec(s.shape, lambda i, me: (0, 0)) for s in shards],
            out_specs=tuple(pl.BlockSpec(s.shape, lambda i, me: (me[0], 0)) for s in shards)),
        compiler_params=_params(dimension_semantics=("arbitrary",)),
    )(me, *shards)


def _gather_start(shards, after, name):
    ns = len(shards)
    zones = _own_blocks_placed(shards, after)

    def body(*refs):
        for cp in _first_hop_copies(shards, refs[:ns], refs[ns:2 * ns], refs[2 * ns], refs[2 * ns + 1]):
            cp.start()
        refs[-1][...] = jnp.zeros_like(refs[-1])

    out = pl.pallas_call(
        body, name=name,
        out_shape=(pltpu.SemaphoreType.DMA((4 * ns,)), pltpu.SemaphoreType.DMA((4 * ns,)),
                   *[pltpu.HBM(a.shape, a.dtype) for a in (*shards, *zones)],
                   jax.ShapeDtypeStruct((8, LANES), F32)),
        in_specs=[HBM] * (2 * ns),
        out_specs=(SEM, SEM, *[HBM] * (2 * ns), pl.BlockSpec(memory_space=pltpu.VMEM)),
        input_output_aliases={i: 2 + i for i in range(2 * ns)},
        compiler_params=pltpu.CompilerParams(has_side_effects=EFFECT),
    )(*[_in_hbm(a) for a in shards], *[_in_hbm(a) for a in zones])
    return out[0], out[1], out[2:2 + ns], out[2 + ns:2 + 2 * ns], out[-1]


def _gather_wait(send_sems, recv_sems, shards, zones, which, after, name):
    ns = len(shards)

    def body(*refs):
        copies = _first_hop_copies(shards, refs[:ns], refs[ns:2 * ns], refs[2 * ns], refs[2 * ns + 1])
        for i, cp in enumerate(copies):
            if i % ns in which:
                cp.wait_send()
                cp.wait_recv()

    out = pl.pallas_call(
        body, name=name,
        out_shape=tuple(pltpu.HBM(a.shape, a.dtype) for a in (*shards, *zones)),
        in_specs=[HBM] * (2 * ns) + [SEM, SEM, ANY],
        out_specs=(HBM,) * (2 * ns),
        input_output_aliases={i: i for i in range(2 * ns)},
        compiler_params=pltpu.CompilerParams(has_side_effects=EFFECT),
    )(*shards, *zones, send_sems, recv_sems, after)
    return out[:ns], out[ns:]


def _gather_pass_on(rows, zones, barrier_id, name):
    ns = len(zones)

    def body(*refs):
        in_refs, out_refs = refs[:ns], refs[ns:2 * ns]
        send_sems, recv_sems = refs[2 * ns:]
        x, y, c = _position()
        _pair_handshake(x, y, c)
        copies = [pltpu.make_async_remote_copy(
            src_ref=_block_of(in_refs[s], 4 * cx + 2 * cy + c, rows[s]),
            dst_ref=_block_of(out_refs[s], 4 * cx + 2 * cy + c, rows[s]),
            send_sem=send_sems.at[j * ns + s], recv_sem=recv_sems.at[j * ns + s],
            device_id=(x, y, 1 - c), device_id_type=MESH_ID)
            for j, (cx, cy) in enumerate(_other_chips(x, y)) for s in range(ns)]
        for cp in copies:
            cp.start()
        for cp in copies:
            cp.wait_recv()
        for cp in copies:
            cp.wait_send()

    return pl.pallas_call(
        body, name=name,
        out_shape=tuple(jax.ShapeDtypeStruct(z.shape, z.dtype) for z in zones),
        in_specs=[ANY] * ns, out_specs=(ANY,) * ns,
        input_output_aliases={i: i for i in range(ns)},
        scratch_shapes=[pltpu.SemaphoreType.DMA((3 * ns,)), pltpu.SemaphoreType.DMA((3 * ns,))],
        compiler_params=pltpu.CompilerParams(collective_id=barrier_id),
    )(*zones)


def _pair_copies(sections, g_refs, land, send_sems, recv_sems):
    ns = len(sections)
    x, y, c = _position()
    return [pltpu.make_async_remote_copy(
        src_ref=_block_of(g_refs[s], 2 * k + 1 - c, rows), dst_ref=land[s].at[k],
        send_sem=send_sems.at[k * ns + s], recv_sem=recv_sems.at[k * ns + s],
        device_id=(x, y, 1 - c), device_id_type=MESH_ID)
        for k in range(N_CHIPS) for s, (_, rows, _) in enumerate(sections)]


def _pair_exchange_start(sections, grads, barrier_id, name):
    ns = len(sections)

    def body(*refs):
        _pair_handshake(*_position())
        for cp in _pair_copies(sections, refs[:ns], refs[ns:2 * ns], refs[2 * ns], refs[2 * ns + 1]):
            cp.start()
        refs[-1][...] = jnp.zeros_like(refs[-1])

    zones = [lax.empty((N_CHIPS, rows, cols), BF16) for _, rows, cols in sections]
    n = N_CHIPS * ns
    out = pl.pallas_call(
        body, name=name,
        out_shape=(pltpu.SemaphoreType.DMA((n,)), pltpu.SemaphoreType.DMA((n,)),
                   *[pltpu.HBM(a.shape, a.dtype) for a in (*grads, *zones)],
                   jax.ShapeDtypeStruct((8, LANES), F32)),
        in_specs=[HBM] * (2 * ns),
        out_specs=(SEM, SEM, *[HBM] * (2 * ns), pl.BlockSpec(memory_space=pltpu.VMEM)),
        input_output_aliases={i: 2 + i for i in range(2 * ns)},
        compiler_params=pltpu.CompilerParams(has_side_effects=EFFECT, collective_id=barrier_id),
    )(*[_in_hbm(a) for a in grads], *[_in_hbm(a) for a in zones])
    return out[0], out[1], out[2:2 + ns], out[2 + ns:2 + 2 * ns], out[-1]


def _pair_exchange_wait(sections, send_sems, recv_sems, grads, zones, after, name):
    ns = len(sections)

    def body(*refs):
        for cp in _pair_copies(sections, refs[:ns], refs[ns:2 * ns], refs[2 * ns], refs[2 * ns + 1]):
            cp.wait_send()
            cp.wait_recv()

    out = pl.pallas_call(
        body, name=name,
        out_shape=tuple(pltpu.HBM(a.shape, a.dtype) for a in (*grads, *zones)),
        in_specs=[HBM] * (2 * ns) + [SEM, SEM, ANY],
        out_specs=(HBM,) * (2 * ns),
        input_output_aliases={i: i for i in range(2 * ns)},
        compiler_params=pltpu.CompilerParams(has_side_effects=EFFECT),
    )(*grads, *zones, send_sems, recv_sems, after)
    return out[:ns], out[ns:]


def _pair_add(sections, grads, got, core, name):
    ns = len(sections)

    def body(core_ref, *refs):
        g_refs, got_refs, p_refs = refs[:ns], refs[ns:2 * ns], refs[2 * ns:]
        for s in range(ns):
            p_refs[s][0] = (g_refs[s][...].astype(F32) + got_refs[s][0].astype(F32)).astype(BF16)

    slot = [pl.BlockSpec((1, rows, cols), lambda k, c: (k, 0, 0)) for _, rows, cols in sections]
    return pl.pallas_call(
        body, name=name,
        out_shape=tuple(jax.ShapeDtypeStruct((N_CHIPS, rows, cols), BF16) for _, rows, cols in sections),
        grid_spec=pltpu.PrefetchScalarGridSpec(
            num_scalar_prefetch=1, grid=(N_CHIPS,),
            in_specs=[pl.BlockSpec((rows, cols), lambda k, c: (2 * k + c[0], 0)) for _, rows, cols in sections]
            + slot,
            out_specs=tuple(slot)),
        compiler_params=_params(dimension_semantics=("parallel",)),
    )(core, *grads, *got)


def _chip_copies(sections, p_refs, land, send_sems, recv_sems):
    ns = len(sections)
    x, y, c = _position()
    return [pltpu.make_async_remote_copy(
        src_ref=p_refs[s].at[2 * cx + cy], dst_ref=land[s].at[j],
        send_sem=send_sems.at[j * ns + s], recv_sem=recv_sems.at[j * ns + s],
        device_id=(cx, cy, c), device_id_type=MESH_ID)
        for j, (cx, cy) in enumerate(_other_chips(x, y)) for s in range(ns)]


def _chip_exchange(sections, parts, name):
    ns = len(sections)

    def body(*refs):
        copies = _chip_copies(sections, refs[:ns], refs[ns:2 * ns], *refs[2 * ns:])
        for cp in copies:
            cp.start()
        for cp in copies:
            cp.wait_recv()
        for cp in copies:
            cp.wait_send()

    n = 3 * ns
    return pl.pallas_call(
        body, name=name,
        out_shape=tuple(jax.ShapeDtypeStruct((3, rows, cols), BF16) for _, rows, cols in sections),
        in_specs=[ANY] * ns, out_specs=(ANY,) * ns,
        scratch_shapes=[pltpu.SemaphoreType.DMA((n,)), pltpu.SemaphoreType.DMA((n,))],
    )(*parts)


def _chip_exchange_start(sections, parts, name):
    ns = len(sections)

    def body(*refs):
        p_refs, land = refs[:ns], refs[ns:2 * ns]
        send_sems, recv_sems = refs[2 * ns], refs[2 * ns + 1]
        token = refs[-1]
        for cp in _chip_copies(sections, p_refs, land, send_sems, recv_sems):
            cp.start()
        token[...] = jnp.zeros_like(token)

    zones = [lax.empty((3, rows, cols), BF16) for _, rows, cols in sections]
    out = pl.pallas_call(
        body, name=name,
        out_shape=(pltpu.SemaphoreType.DMA((3 * ns,)), pltpu.SemaphoreType.DMA((3 * ns,)),
                   *[pltpu.HBM(a.shape, a.dtype) for a in parts], *[pltpu.HBM(a.shape, a.dtype) for a in zones],
                   jax.ShapeDtypeStruct((8, LANES), F32)),
        in_specs=[HBM] * (2 * ns),
        out_specs=(SEM, SEM, *[HBM] * (2 * ns), pl.BlockSpec(memory_space=pltpu.VMEM)),
        input_output_aliases={i: 2 + i for i in range(2 * ns)},
        compiler_params=pltpu.CompilerParams(has_side_effects=EFFECT),
    )(*[_in_hbm(a) for a in parts], *[_in_hbm(a) for a in zones])
    return out[0], out[1], out[2:2 + ns], out[2 + ns:2 + 2 * ns], out[-1]


def _chip_exchange_wait(sections, send_sems, recv_sems, parts, zones, after, name):
    ns = len(sections)

    def body(*refs):
        p_refs, land = refs[:ns], refs[ns:2 * ns]
        for cp in _chip_copies(sections, p_refs, land, refs[2 * ns], refs[2 * ns + 1]):
            cp.wait_send()
            cp.wait_recv()

    out = pl.pallas_call(
        body, name=name,
        out_shape=tuple(pltpu.HBM(a.shape, a.dtype) for a in (*parts, *zones)),
        in_specs=[HBM] * (2 * ns) + [SEM, SEM, ANY],
        out_specs=(HBM,) * (2 * ns),
        input_output_aliases={i: i for i in range(2 * ns)},
        compiler_params=pltpu.CompilerParams(has_side_effects=EFFECT),
    )(*parts, *zones, send_sems, recv_sems, after)
    return out[:ns], out[ns:]


def _grad_finish(sections, parts, far, chip, name):
    ns = len(sections)

    def body(chip_ref, *refs):
        p_refs, b_refs, g_refs = refs[:ns], refs[ns:2 * ns], refs[2 * ns:]
        for s in range(ns):
            g = p_refs[s][0].astype(F32)
            for j in range(3):
                g = g + b_refs[s][j].astype(F32)
            g_refs[s][...] = g

    half = [(rows // 2, cols) for _, rows, cols in sections]
    return pl.pallas_call(
        body, name=name,
        out_shape=tuple(jax.ShapeDtypeStruct((rows, cols), F32) for _, rows, cols in sections),
        grid_spec=pltpu.PrefetchScalarGridSpec(
            num_scalar_prefetch=1, grid=(2,),
            in_specs=[pl.BlockSpec((1, r, c), lambda i, chip: (chip[0], i, 0)) for r, c in half]
            + [pl.BlockSpec((3, r, c), lambda i, chip: (0, i, 0)) for r, c in half],
            out_specs=tuple(pl.BlockSpec((r, c), lambda i, chip: (i, 0)) for r, c in half)),
        compiler_params=_params(dimension_semantics=("parallel",)),
    )(chip, *parts, *far)


def _sum_devices(parts, rows, name):
    cols = parts.shape[1]
    tr = rows // 2

    def body(*refs):
        s = refs[0][...].astype(F32)
        for d in range(1, N_DEV):
            s = s + refs[d][...].astype(F32)
        refs[N_DEV][...] = s

    return pl.pallas_call(
        body, name=name,
        out_shape=jax.ShapeDtypeStruct((rows, cols), F32),
        grid=(2,),
        in_specs=[pl.BlockSpec((tr, cols), lambda i, d=d: (2 * d + i, 0)) for d in range(N_DEV)],
        out_specs=pl.BlockSpec((tr, cols), lambda i: (i, 0)),
        compiler_params=_params(dimension_semantics=("parallel",)),
    )(*([parts] * N_DEV))


def _adamw_step(w_ref, g_ref, m_ref, v_ref, d_ref, nm_ref, nv_ref):
    c1 = 1.0 / (1.0 - ADAM_B1 ** ADAM_STEP)
    c2 = 1.0 / (1.0 - ADAM_B2 ** ADAM_STEP)
    gv = g_ref[...]
    nm = ADAM_B1 * m_ref[...] + (1.0 - ADAM_B1) * gv
    nv = ADAM_B2 * v_ref[...] + (1.0 - ADAM_B2) * (gv * gv)
    nm_ref[...] = nm
    nv_ref[...] = nv
    d_ref[...] = (-ADAM_LR) * ((nm * c1) / (jnp.sqrt(nv * c2) + ADAM_EPS) + ADAM_WD * w_ref[...])


def _adamw_small(params, name):
    n = len(params)

    def body(*refs):
        for k in range(n):
            _adamw_step(*refs[4 * k:4 * k + 4], *refs[4 * n + 3 * k:4 * n + 3 * k + 3])

    out = pl.pallas_call(
        body, name=name,
        out_shape=tuple(jax.ShapeDtypeStruct(p[0].shape, F32) for p in params for _ in range(3)),
    )(*[a for p in params for a in p])
    return [out[3 * k:3 * k + 3] for k in range(n)]


def _adamw(w, g, m, v, name):
    rows, cols = w.shape
    tr = rows
    while tr * cols * 4 > (1 << 20) and tr % 16 == 0:
        tr //= 2

    def body(*refs):
        _adamw_step(*refs)

    spec = pl.BlockSpec((tr, cols), lambda i: (i, 0))
    shape = jax.ShapeDtypeStruct((rows, cols), F32)
    return pl.pallas_call(
        body, name=name,
        out_shape=(shape, shape, shape),
        grid=(rows // tr,),
        in_specs=[spec] * 4, out_specs=(spec,) * 3,
        compiler_params=_params(dimension_semantics=("parallel",)),
    )(w, g, m, v)


NAMES = ("ln1_g", "w_in", "b_in", "rpb", "w_att_o", "conv_w", "conv_b", "w_rg_a", "b_rg_a", "w_rg_i",
         "b_rg_i", "lru_lambda", "w_rec_o", "w_out", "ln2_g", "w_ff1", "w_ff2", "lnf_g")
TRANSPOSED = {"w_in": "w_in_t", "w_att_o": "w_att_o_t", "w_ff1": "w_ff1_t"}
ROW_SHARDED = ("w_rec_o", "w_out", "w_ff2")
REPLICATED = (("ln1_g", (1, D)), ("b_in", (1, D_IN)), ("rpb", (N_HEADS * N_RPB_R, N_RPB_C)),
              ("conv_b", (1, D_REC)), ("w_rg_a", (2 * N_REC_BLOCKS * REC_BLOCK, REC_BLOCK)),
              ("w_rg_i", (2 * N_REC_BLOCKS * REC_BLOCK, REC_BLOCK)), ("ln2_g", (1, D)), ("lnf_g", (1, D)))
GATE_BLOCKS = ("w_rg_a", "w_rg_i")
SMALL_ROWS = 112


def _chan_bits(vectors):
    chan = jnp.concatenate(vectors, axis=0)
    bits = lax.bitcast_convert_type(chan, BF16).reshape(-1)
    return jnp.pad(bits, (0, CHAN_BLOCK_ROWS * D - bits.shape[0])).reshape(CHAN_BLOCK_ROWS, D)


def _chan_from_bits(gathered):
    bits = gathered.reshape(N_DEV, CHAN_BLOCK_ROWS * D)[:, :2 * N_CHAN_ROWS * LANES]
    chan = lax.bitcast_convert_type(bits.reshape(N_DEV, N_CHAN_ROWS, LANES, 2), F32)
    return chan.transpose(1, 0, 2).reshape(N_CHAN_ROWS, D)


def kernel(x, ln1_g, w_in, b_in, rpb, w_att_o, conv_w, conv_b, w_rg_a, b_rg_a, w_rg_i, b_rg_i, lru_lambda, w_rec_o, w_out, ln2_g, w_ff1, w_ff2, lnf_g, loss_target, m_ln1_g, m_w_in, m_b_in, m_rpb, m_w_att_o, m_conv_w, m_conv_b, m_w_rg_a, m_b_rg_a, m_w_rg_i, m_b_rg_i, m_lru_lambda, m_w_rec_o, m_w_out, m_ln2_g, m_w_ff1, m_w_ff2, m_lnf_g, v_ln1_g, v_w_in, v_b_in, v_rpb, v_w_att_o, v_conv_w, v_conv_b, v_w_rg_a, v_b_rg_a, v_w_rg_i, v_b_rg_i, v_lru_lambda, v_w_rec_o, v_w_out, v_ln2_g, v_w_ff1, v_w_ff2, v_lnf_g):
    w = dict(zip(NAMES, (ln1_g, w_in, b_in, rpb, w_att_o, conv_w, conv_b, w_rg_a, b_rg_a, w_rg_i,
                         b_rg_i, lru_lambda, w_rec_o, w_out, ln2_g, w_ff1, w_ff2, lnf_g)))
    m = dict(zip(NAMES, (m_ln1_g, m_w_in, m_b_in, m_rpb, m_w_att_o, m_conv_w, m_conv_b, m_w_rg_a,
                         m_b_rg_a, m_w_rg_i, m_b_rg_i, m_lru_lambda, m_w_rec_o, m_w_out, m_ln2_g,
                         m_w_ff1, m_w_ff2, m_lnf_g)))
    v = dict(zip(NAMES, (v_ln1_g, v_w_in, v_b_in, v_rpb, v_w_att_o, v_conv_w, v_conv_b, v_w_rg_a,
                         v_b_rg_a, v_w_rg_i, v_b_rg_i, v_lru_lambda, v_w_rec_o, v_w_out, v_ln2_g,
                         v_w_ff1, v_w_ff2, v_lnf_g)))
    xi, yi, ci = _position()

    shard = {t: w[n][0].T.astype(BF16) for n, t in TRANSPOSED.items()}
    shard.update({n: w[n][0].astype(BF16) for n in ROW_SHARDED})
    shard["chan"] = _chan_bits([w[n][0] for n, _ in CHAN])
    first, later = ("w_in_t", "chan"), ("w_rec_o", "w_out", "w_att_o_t", "w_ff1_t", "w_ff2")
    *gathered, done = _all_gather([shard[n] for n in first], "weight_all_gather")
    p = dict(zip(first, gathered))
    send_sems, recv_sems, sent, zones, token = _gather_start([shard[n] for n in later], done,
                                                             "weight_gather_start")

    travelling = {"shards": sent, "zones": zones}
    stages = (("w_rec_o", "w_out", "w_att_o_t"), ("w_ff1_t", "w_ff2"))

    def late_weights(after, stage):
        which = [later.index(n) for n in stages[stage]]
        travelling["shards"], travelling["zones"] = _gather_wait(
            send_sems, recv_sems, travelling["shards"], travelling["zones"], which, after,
            "weight_gather_wait_%d" % stage)
        return dict(zip(stages[stage], _gather_pass_on(
            [shard[n].shape[0] for n in stages[stage]], [travelling["zones"][i] for i in which],
            PASS_ON_IDS[stage], "weight_gather_pass_on_%d" % stage)))

    chan = _chan_from_bits(p.pop("chan"))
    r0 = 0
    for n, rows in CHAN:
        p[n] = chan[r0:r0 + rows]
        r0 += rows
    p.update(ln1_g=w["ln1_g"], b_in=w["b_in"] + token[0, 0], rpb=w["rpb"][0], conv_b=w["conv_b"],
             w_rg_a=w["w_rg_a"][0], w_rg_i=w["w_rg_i"][0], ln2_g=w["ln2_g"],
             lnf_g=w["lnf_g"].reshape(1, D))

    core = jnp.reshape(ci, (1,)).astype(jnp.int32)
    chip = jnp.reshape(2 * xi + yi, (1,)).astype(jnp.int32)
    first_sections = tuple(s for s in SECTIONS if s[0] in ("w_ff1_t", "w_ff2"))
    late_sections = SECTIONS[:1]
    early_sections = tuple(s for s in SECTIONS[1:] if s not in first_sections)
    in_flight = {}

    def pair_sum_and_send(group, sections, after):
        send_sems, recv_sems, sect, zones, _ = in_flight["pair_" + group]
        sect, got = _pair_exchange_wait(sections, send_sems, recv_sems, sect, zones, after,
                                        "grad_pair_exchange_wait_" + group)
        parts = _pair_add(sections, sect, got, core, "grad_pair_add_" + group)
        in_flight[group] = _chip_exchange_start(sections, parts, "grad_chip_exchange_start_" + group)
        return in_flight[group][-1]

    def pair_exchange_at_once(group, sections, grads, barrier_id):
        in_flight["pair_" + group] = _pair_exchange_start(
            sections, [grads[n] for n, _, _ in sections], barrier_id, "grad_pair_exchange_start_" + group)
        return pair_sum_and_send(group, sections, in_flight["pair_" + group][-1])

    def reduce_first(grads, after):
        if grads is None:
            return pair_sum_and_send("first", first_sections, after)
        in_flight["pair_first"] = _pair_exchange_start(
            first_sections, [grads[n] for n, _, _ in first_sections], PAIR_FIRST_ID,
            "grad_pair_exchange_start_first")
        return in_flight["pair_first"][-1]

    def reduce_early(grads):
        chan_g = jnp.concatenate([grads[n] for n, _ in CHAN], axis=0)
        chan_g = chan_g.reshape(N_CHAN_ROWS, N_DEV, LANES).transpose(1, 0, 2).astype(BF16)
        chan_g = jnp.pad(chan_g.reshape(N_DEV, -1), ((0, 0), (0, CHAN_BLOCK_ROWS * D - N_CHAN_ROWS * LANES)))
        grads["chan"] = chan_g.reshape(N_DEV * CHAN_BLOCK_ROWS, D)
        grads["gates"] = jnp.concatenate([grads[n].reshape(-1, D) for n in GATE_BLOCKS], axis=0).astype(BF16)
        return pair_exchange_at_once("early", early_sections, grads, PAIR_EARLY_ID)[0, 0]

    def finish(group, sections, after, name):
        send_sems, recv_sems, parts, zones, _ = in_flight[group]
        parts, far = _chip_exchange_wait(sections, send_sems, recv_sems, parts, zones, after,
                                         "grad_chip_exchange_wait_" + name)
        return dict(zip((n for n, _, _ in sections),
                        _grad_finish(sections, parts, far, chip, "grad_finish_" + name)))

    summed = {}

    def reduce_late(grads):
        in_flight["pair_late"] = _pair_exchange_start(
            late_sections, [grads[n] for n, _, _ in late_sections], PAIR_LATE_ID,
            "grad_pair_exchange_start_late")
        summed.update(finish("first", first_sections, in_flight["pair_late"][-1], "first"))
        summed.update(finish("early", early_sections, summed["w_ff2"], "early"))
        return pair_sum_and_send("late", late_sections, summed["gates"])

    loss_part, grad_x, grads = _local_step(x[0], loss_target[0], p, late_weights, reduce_first, reduce_early,
                                           reduce_late)

    flat = jnp.concatenate([grads[n].reshape(-1) for n, _ in REPLICATED if n not in GATE_BLOCKS]
                           + [loss_part.reshape(-1)])
    n_small = flat.shape[0]
    flat = jnp.pad(flat, (0, SMALL_ROWS * LANES - n_small)).reshape(SMALL_ROWS, LANES)
    small_parts, gate_sum, _ = _all_gather([flat, summed["gates"]], "small_grad_all_gather", relay=False)
    small = _sum_devices(small_parts, SMALL_ROWS, "small_grad_sum").reshape(-1)
    loss = small[n_small - 1]

    g, delta, new_m, new_v = {}, {}, {}, {}

    def update(n, g2, shape2):
        d2, m2, v2 = _adamw(w[n].reshape(shape2), g2, m[n].reshape(shape2), v[n].reshape(shape2),
                            "adamw_" + n)
        g[n], delta[n], new_m[n], new_v[n] = (a.reshape(w[n].shape) for a in (g2, d2, m2, v2))

    small_params = []
    o = 0
    for n, shape2 in REPLICATED:
        if n in GATE_BLOCKS:
            k, rows = GATE_BLOCKS.index(n), gate_sum.shape[0] // len(GATE_BLOCKS)
            update(n, gate_sum[k * rows:(k + 1) * rows].reshape(shape2), shape2)
        else:
            size = shape2[0] * shape2[1]
            small_params.append((n, small[o:o + size].reshape(shape2), shape2))
            o += size
    chan_back = summed["chan"].reshape(-1)[:N_CHAN_ROWS * LANES].reshape(N_CHAN_ROWS, LANES)
    r0 = 0
    for n, rows in CHAN:
        small_params.append((n, chan_back[r0:r0 + rows], (rows, LANES)))
        r0 += rows
    results = _adamw_small([(w[n].reshape(s2), g2, m[n].reshape(s2), v[n].reshape(s2))
                            for n, g2, s2 in small_params], "adamw_vectors")
    for (n, g2, _), (d2, m2, v2) in zip(small_params, results):
        g[n], delta[n], new_m[n], new_v[n] = (a.reshape(w[n].shape) for a in (g2, d2, m2, v2))

    for n in ROW_SHARDED:
        update(n, summed[n], summed[n].shape)
    for n, t in TRANSPOSED.items():
        if t in summed:
            update(n, summed[t].T, summed[t].shape[::-1])
    summed = finish("late", late_sections, _after_all(list(delta.values()), "updates_done"), "late")
    g_t = summed["w_in_t"]
    results = _adamw(w["w_in"][0].T, g_t, m["w_in"][0].T, v["w_in"][0].T, "adamw_w_in")
    g["w_in"], delta["w_in"], new_m["w_in"], new_v["w_in"] = (a.T[None] for a in (g_t, *results))

    return (loss, grad_x[None], *[g[n] for n in NAMES], *[delta[n] for n in NAMES],
            *[new_m[n] for n in NAMES], *[new_v[n] for n in NAMES])
```

```python
import math

import numpy as np
import jax
import jax.numpy as jnp
from jax import lax
from jax.experimental import pallas as pl
from jax.experimental.pallas import tpu as pltpu

F32 = jnp.float32
BF16 = jnp.bfloat16

T = 2048
D = 1024
D_ATT = 512
D_REC = 1024
D_FF = 4096
D_IN = 5632
N_HEADS = 8
DH = 64
GRID_W = 64
ROWS = T // GRID_W
WIN_H = 8
WIN_W = 16
KWIN = WIN_H * GRID_W
N_RPB_R = 2 * WIN_H - 1
N_RPB_C = 2 * WIN_W - 1
N_REC_BLOCKS = 16
REC_BLOCK = 64
CG = 128
N_CG = D_REC // CG
LRU_C = 8.0
EPS = 1e-6
N_DEV = 8
N_CHIPS = 4
LANES = 128

ADAM_LR = 0.001
ADAM_B1 = 0.9
ADAM_B2 = 0.999
ADAM_EPS = 1e-08
ADAM_WD = 0.01
ADAM_STEP = 10

MESH_AXES = ("x", "y", "c")
VMEM_LIMIT = 56 * 1024 * 1024

TILE = 512
DZ_ARRAYS = ((0, 3, 1), (3, 4, 2), (7, 4, 2))
N_DZ_TILES = D_IN // TILE


def _params(**kw):
    return pltpu.CompilerParams(vmem_limit_bytes=VMEM_LIMIT, **kw)


HG = 4
HQ = HG * GRID_W
HC = HG * DH


def _att_tables():
    rq = np.arange(GRID_W)
    kc = np.arange(KWIN) % GRID_W
    win_start = np.clip(rq - WIN_W // 2, 0, GRID_W - WIN_W)
    valid = (kc[None, :] >= win_start[:, None]) & (kc[None, :] < win_start[:, None] + WIN_W)
    same_head = (np.arange(HQ)[:, None] // GRID_W) == (np.arange(HC)[None, :] // DH)
    return valid.astype(np.float32), same_head.astype(np.float32)


def _pair_mask():
    half = np.arange(2 * DH) // DH
    return (half[:, None] == half[None, :]).astype(np.float32)


def _dup_table():
    return np.concatenate([np.eye(REC_BLOCK, dtype=np.float32)] * 2, axis=1)


def _sigmoid(x):
    return 0.5 * jnp.tanh(0.5 * x) + 0.5


def _softplus(x):
    return jnp.maximum(x, 0.0) + jnp.log(1.0 + jnp.exp(-jnp.abs(x)))


def _one_minus_square(log_a, a):
    x = 2.0 * log_a
    series = -x * (1.0 + x * (0.5 + x * (1.0 / 6.0)))
    return jnp.where(x > -0.02, series, 1.0 - a * a)


_GELU_C = math.sqrt(2.0 / math.pi)


def _gelu_and_grad(x):
    x2 = x * x
    inner = _GELU_C * (x + 0.044715 * x * x2)
    t = jnp.tanh(inner)
    g = 0.5 * x * (1.0 + t)
    dg = 0.5 * (1.0 + t) + 0.5 * x * (1.0 - t * t) * _GELU_C * (1.0 + 3.0 * 0.044715 * x2)
    return g, dg


def _dot(a, b):
    return jnp.dot(a, b, preferred_element_type=F32)


def _dot_nt(a, b):
    return lax.dot_general(a, b, (((1,), (1,)), ((), ())), preferred_element_type=F32)


def _dot_tn(a, b):
    return lax.dot_general(a, b, (((0,), (0,)), ((), ())), preferred_element_type=F32)


def _dot_exact(a, b):
    return jnp.dot(a, b, precision=lax.Precision.HIGHEST, preferred_element_type=F32)


def _shift_rows(x, s):
    n = x.shape[0]
    rows = lax.broadcasted_iota(jnp.int32, x.shape, 0)
    y = pltpu.roll(x, s % n, 0)
    if s > 0:
        return jnp.where(rows >= s, y, 0.0)
    return jnp.where(rows < n + s, y, 0.0)


def _rms_bwd(dh, xh, r, g):
    dxh = dh * g
    return r * (dxh - xh * jnp.mean(dxh * xh, axis=-1, keepdims=True))


def _matmul(a, b, mode, out_dtype, name, tm=512, tn=1024, tk=2048):
    if mode == "nn":
        (m, k), (k2, n) = a.shape, b.shape
    elif mode == "nt":
        (m, k), (n, k2) = a.shape, b.shape
    else:
        (k, m), (k2, n) = a.shape, b.shape
    assert k == k2
    tm, tn, tk = min(tm, m), min(tn, n), min(tk, k)
    assert m % tm == 0 and n % tn == 0 and k % tk == 0
    nk = k // tk
    dot = {"nn": _dot, "nt": _dot_nt, "tn": _dot_tn}[mode]

    def body(a_ref, b_ref, o_ref, acc):
        kk = pl.program_id(2)
        part = dot(a_ref[...].astype(BF16), b_ref[...].astype(BF16))
        if nk == 1:
            o_ref[...] = part.astype(out_dtype)
            return

        @pl.when(kk == 0)
        def _():
            acc[...] = part

        @pl.when(kk > 0)
        def _():
            acc[...] += part

        @pl.when(kk == nk - 1)
        def _():
            o_ref[...] = acc[...].astype(out_dtype)

    if mode == "tn":
        a_spec = pl.BlockSpec((tk, tm), lambda i, j, kk: (kk, i))
    else:
        a_spec = pl.BlockSpec((tm, tk), lambda i, j, kk: (i, kk))
    if mode == "nt":
        b_spec = pl.BlockSpec((tn, tk), lambda i, j, kk: (j, kk))
    else:
        b_spec = pl.BlockSpec((tk, tn), lambda i, j, kk: (kk, j))
    return pl.pallas_call(
        body, name=name,
        out_shape=jax.ShapeDtypeStruct((m, n), out_dtype),
        grid=(m // tm, n // tn, nk),
        in_specs=[a_spec, b_spec],
        out_specs=pl.BlockSpec((tm, tn), lambda i, j, kk: (i, j)),
        scratch_shapes=[pltpu.VMEM((tm, tn) if nk > 1 else (8, LANES), F32)],
        compiler_params=_params(dimension_semantics=("parallel", "parallel", "arbitrary")),
    )(a, b)


def _in_proj(x, g1, w_in_t, b_in):
    tm = 512

    def body(x_ref, g_ref, w_hbm, b_ref, qkv_ref, uy_ref, gg_ref, h_ref, w):
        @pl.when(pl.program_id(0) == 0)
        def _():
            pltpu.sync_copy(w_hbm, w)

        xv = x_ref[...]
        r = lax.rsqrt(jnp.mean(xv * xv, axis=-1, keepdims=True) + EPS)
        h = ((xv * r) * g_ref[...]).astype(BF16)
        h_ref[...] = h
        row0 = 0
        for ref in (qkv_ref, uy_ref, gg_ref):
            for c0 in range(0, ref.shape[1], TILE):
                z = _dot_nt(h, w[row0:row0 + TILE, :]) + b_ref[:, row0:row0 + TILE]
                ref[:, c0:c0 + TILE] = z.astype(ref.dtype)
                row0 += TILE

    tok = lambda width: pl.BlockSpec((tm, width), lambda i: (i, 0))
    return pl.pallas_call(
        body, name="in_proj",
        out_shape=(jax.ShapeDtypeStruct((T, 3 * D_ATT), BF16),
                   jax.ShapeDtypeStruct((T, 2 * D_REC), F32),
                   jax.ShapeDtypeStruct((T, 2 * D), F32),
                   jax.ShapeDtypeStruct((T, D), BF16)),
        grid=(T // tm,),
        in_specs=[tok(D), pl.BlockSpec((1, D), lambda i: (0, 0)), pl.BlockSpec(memory_space=pl.ANY),
                  pl.BlockSpec((1, D_IN), lambda i: (0, 0))],
        out_specs=(tok(3 * D_ATT), tok(2 * D_REC), tok(2 * D), tok(D)),
        scratch_shapes=[pltpu.VMEM((D_IN, D), BF16)],
        compiler_params=_params(dimension_semantics=("arbitrary",)),
    )(x, g1, w_in_t, b_in)


def _dz_specs(rows, tile_of, row_of):
    def spec(off, n, per_plane):
        def index(*ids):
            t = jnp.clip(tile_of(*ids) - off, 0, n - 1)
            return (t // per_plane, row_of(*ids), t % per_plane)
        return pl.BlockSpec((1, rows, TILE), index)
    return [spec(off, n, per) for off, n, per in DZ_ARRAYS]


def _dh_norm1_bwd(dz, w_in_t, x, g1, dx1, after):
    tm = 512

    def body(dqkv_ref, duy_ref, dgg_ref, w_hbm, x_ref, g_ref, dx1_ref, after_ref, gx_ref, dg_ref, w):
        @pl.when(pl.program_id(0) == 0)
        def _():
            pltpu.sync_copy(w_hbm, w)
            dg_ref[...] = jnp.zeros_like(dg_ref)

        dh, row0 = None, 0
        for ref in (dqkv_ref, duy_ref, dgg_ref):
            for plane in range(ref.shape[0]):
                cols = ref.shape[2]
                part = _dot(ref[plane], w[row0:row0 + cols, :])
                dh = part if dh is None else dh + part
                row0 += cols
        xv = x_ref[...]
        r = lax.rsqrt(jnp.mean(xv * xv, axis=-1, keepdims=True) + EPS)
        xh = xv * r
        dg_ref[...] += jnp.sum(dh * xh, axis=0, keepdims=True)
        gx_ref[...] = dx1_ref[...] + _rms_bwd(dh, xh, r, g_ref[...])

    tok = pl.BlockSpec((tm, D), lambda i: (i, 0))
    vec = pl.BlockSpec((1, D), lambda i: (0, 0))
    planes = lambda a: pl.BlockSpec((a.shape[0], tm, a.shape[2]), lambda i: (0, i, 0))
    return pl.pallas_call(
        body, name="dh_norm1_bwd",
        out_shape=(jax.ShapeDtypeStruct((T, D), F32), jax.ShapeDtypeStruct((1, D), F32)),
        grid=(T // tm,),
        in_specs=[planes(a) for a in dz] + [pl.BlockSpec(memory_space=pl.ANY), tok, vec, tok,
                                            pl.BlockSpec(memory_space=pl.ANY)],
        out_specs=(tok, vec),
        scratch_shapes=[pltpu.VMEM((D_IN, D), BF16)],
        compiler_params=_params(dimension_semantics=("arbitrary",)),
    )(*dz, w_in_t, x, g1, dx1, after)


def _grad_w_in(dz, h):
    def body(*refs):
        seg_refs = refs[:3]
        h_ref, gw_ref, gb_ref = refs[3:]
        j = pl.program_id(0)

        for s, (off, n, _) in enumerate(DZ_ARRAYS):
            @pl.when((j >= off) & (j < off + n))
            def _(s=s):
                a = seg_refs[s][0]
                gw_ref[...] = _dot_tn(a, h_ref[...]).astype(BF16)
                gb_ref[...] = jnp.sum(a.astype(F32), axis=0, keepdims=True)

    return pl.pallas_call(
        body, name="grad_w_in",
        out_shape=(jax.ShapeDtypeStruct((D_IN, D), BF16), jax.ShapeDtypeStruct((1, D_IN), F32)),
        grid=(N_DZ_TILES,),
        in_specs=_dz_specs(T, lambda j: j, lambda j: 0) + [pl.BlockSpec((T, D), lambda j: (0, 0))],
        out_specs=(pl.BlockSpec((TILE, D), lambda j: (j, 0)), pl.BlockSpec((1, TILE), lambda j: (0, j))),
        compiler_params=_params(dimension_semantics=("parallel",)),
    )(*dz, h)


def _rpb_rows(rpb):
    padded = jnp.pad(rpb, ((0, 0), (0, 0), (0, GRID_W - N_RPB_C)))
    rows = [padded[:, WIN_H - 1 - oi: 2 * WIN_H - 1 - oi].reshape(N_HEADS // HG, HG, KWIN)
            for oi in range(WIN_H)]
    return jnp.stack(rows, axis=0)


SKEW = KWIN - (WIN_W - 1)


MASKED = -1e30


def _bias_tiles(rows_ref, valid, bias_s):
    for oi in range(WIN_H):
        for hh in range(HG):
            row = jnp.broadcast_to(rows_ref[oi, 0, hh:hh + 1, :], (GRID_W, KWIN))
            tile = pltpu.roll(row, SKEW, 1, stride=1, stride_axis=0)
            bias_s[oi, hh * GRID_W:(hh + 1) * GRID_W, :] = jnp.where(valid, tile, MASKED)


def _bias_tile_grads(gb_s, flip, out_ref):
    for oi in range(WIN_H):
        for hh in range(HG):
            g = _dot_exact(flip, gb_s[oi, hh * GRID_W:(hh + 1) * GRID_W, :])
            back = pltpu.roll(g, KWIN - (GRID_W - WIN_W), 1, stride=1, stride_axis=0)
            out_ref[0, oi, hh:hh + 1, :] = jnp.sum(back, axis=0, keepdims=True)


def _rpb_fold(row_grads):
    g = row_grads.transpose(1, 0, 2, 3).reshape(WIN_H, N_HEADS, WIN_H, GRID_W)
    g = g.transpose(0, 2, 1, 3)

    def body(g_ref, o_ref):
        for dr in range(N_RPB_R):
            terms = [g_ref[oi, i] for oi in range(WIN_H) for i in range(WIN_H) if i - oi + WIN_H - 1 == dr]
            acc = terms[0]
            for term in terms[1:]:
                acc = acc + term
            o_ref[dr] = acc

    out = pl.pallas_call(
        body, name="rpb_fold",
        out_shape=jax.ShapeDtypeStruct((N_RPB_R, N_HEADS, GRID_W), F32),
    )(g)
    return out.transpose(1, 0, 2)[:, :, :N_RPB_C]


ATT_GROUPS = N_HEADS // HG
ATT_UNROLL = 8


def _stacked(rows64, same_head):
    return jnp.where(same_head, jnp.concatenate([rows64] * HG, axis=0), jnp.zeros((), BF16))


def _own_heads(stacked):
    head = lax.broadcasted_iota(jnp.int32, (GRID_W, HC), 1) // DH
    out = stacked[:GRID_W]
    for h in range(1, HG):
        out = jnp.where(head == h, stacked[h * GRID_W:(h + 1) * GRID_W], out)
    return out


def _att_scores(q_ref, k_ref, bias_ref, same_head, r):
    rs = jnp.clip(r - WIN_H // 2, 0, ROWS - WIN_H)
    oi = r - rs
    q0 = pl.multiple_of(r * GRID_W, GRID_W)
    k0 = pl.multiple_of(rs * GRID_W, GRID_W)
    q2 = _stacked(q_ref[pl.ds(q0, GRID_W), :] * (DH ** -0.5), same_head)
    kw = k_ref[pl.ds(k0, KWIN), :]
    s = _dot_nt(q2, kw) + bias_ref[oi]
    e = jnp.exp(s - jnp.max(s, axis=-1, keepdims=True))
    return e, 1.0 / jnp.sum(e, axis=-1, keepdims=True), q2, kw, q0, k0, oi


def _att_specs():
    col = lambda off: pl.BlockSpec((T, HC), lambda g: (0, g + off * ATT_GROUPS))
    tables = [pl.BlockSpec((WIN_H, 1, HG, KWIN), lambda g: (0, g, 0, 0)),
              pl.BlockSpec((GRID_W, KWIN), lambda g: (0, 0)),
              pl.BlockSpec((HQ, HC), lambda g: (0, 0))]
    return col, tables, pltpu.VMEM((WIN_H, HQ, KWIN), F32)


def _att_fwd(qkv, bias_rows):
    valid_np, same_head_np = _att_tables()

    def body(q_ref, k_ref, v_ref, rows_ref, valid_ref, head_ref, o_ref, bias_s):
        same_head = head_ref[...] > 0.5
        _bias_tiles(rows_ref, valid_ref[...] > 0.5, bias_s)

        def row(r, carry):
            e, rl, _, _, q0, k0, _ = _att_scores(q_ref, k_ref, bias_s, same_head, r)
            o2 = _dot((e * rl).astype(BF16), v_ref[pl.ds(k0, KWIN), :])
            o_ref[pl.ds(q0, GRID_W), :] = _own_heads(o2).astype(BF16)
            return carry

        lax.fori_loop(0, ROWS, row, 0, unroll=ATT_UNROLL)

    col, tables, tiles = _att_specs()
    return pl.pallas_call(
        body, name="att_fwd",
        out_shape=jax.ShapeDtypeStruct((T, D_ATT), BF16),
        grid=(ATT_GROUPS,),
        in_specs=[col(0), col(1), col(2)] + tables,
        out_specs=col(0),
        scratch_shapes=[tiles],
        compiler_params=_params(dimension_semantics=("parallel",)),
    )(qkv, qkv, qkv, bias_rows, jnp.asarray(valid_np), jnp.asarray(same_head_np))


def _att_bwd(qkv, bias_rows, datt, after):
    valid_np, same_head_np = _att_tables()

    def body(q_ref, k_ref, v_ref, do_ref, rows_ref, valid_ref, head_ref, flip_ref,
             dqkv_ref, grows_ref, dk_acc, dv_acc, bias_s, gb_s):
        same_head = head_ref[...] > 0.5
        dk_acc[...] = jnp.zeros_like(dk_acc)
        dv_acc[...] = jnp.zeros_like(dv_acc)
        gb_s[...] = jnp.zeros_like(gb_s)
        _bias_tiles(rows_ref, valid_ref[...] > 0.5, bias_s)

        def row(r, carry):
            e, rl, q2, kw, q0, k0, oi = _att_scores(q_ref, k_ref, bias_s, same_head, r)
            do2 = _stacked(do_ref[pl.ds(q0, GRID_W), :], same_head)
            vw = v_ref[pl.ds(k0, KWIN), :]
            p = e * rl
            dp = _dot_nt(do2, vw)
            ds = p * (dp - jnp.sum(dp * p, axis=-1, keepdims=True))
            p16 = p.astype(BF16)
            ds16 = ds.astype(BF16)
            dv_acc[pl.ds(k0, KWIN), :] += _dot_tn(p16, do2)
            dk_acc[pl.ds(k0, KWIN), :] += _dot_tn(ds16, q2)
            dq2 = _dot(ds16, kw) * (DH ** -0.5)
            dqkv_ref[0, pl.ds(q0, GRID_W), :] = _own_heads(dq2).astype(BF16)
            gb_s[oi] += ds
            return carry

        lax.fori_loop(0, ROWS, row, 0, unroll=ATT_UNROLL)
        dqkv_ref[1] = dk_acc[...].astype(BF16)
        dqkv_ref[2] = dv_acc[...].astype(BF16)
        _bias_tile_grads(gb_s, flip_ref[...], grows_ref)

    col, tables, tiles = _att_specs()
    return pl.pallas_call(
        body, name="att_bwd",
        out_shape=(jax.ShapeDtypeStruct((3, T, D_ATT), BF16),
                   jax.ShapeDtypeStruct((ATT_GROUPS, WIN_H, HG, KWIN), F32)),
        grid=(ATT_GROUPS,),
        in_specs=[col(0), col(1), col(2), col(0)] + tables + [pl.BlockSpec((GRID_W, GRID_W), lambda g: (0, 0))],
        out_specs=(pl.BlockSpec((3, T, HC), lambda g: (0, 0, g)),
                   pl.BlockSpec((1, WIN_H, HG, KWIN), lambda g: (g, 0, 0, 0))),
        scratch_shapes=[pltpu.VMEM((T, HC), F32), pltpu.VMEM((T, HC), F32), tiles, tiles],
        compiler_params=_params(dimension_semantics=("parallel",)),
    )(qkv, qkv, qkv, datt, bias_rows, jnp.asarray(valid_np) + after, jnp.asarray(same_head_np),
      jnp.asarray(np.eye(GRID_W, dtype=np.float32)[::-1].copy()))


def _conv_taps(up):
    return (_shift_rows(up, 2), _shift_rows(up, 1), up, _shift_rows(up, -1))


def _pair_block_diag(w_pair, dup, same_half):
    return jnp.where(same_half, _dot(w_pair.astype(BF16), dup), 0.0).astype(BF16)


def _gates(u, u16, wa, ba, wi, bi, lam):
    r = _sigmoid(_dot(u16, wa) + ba)
    ig = _sigmoid(_dot(u16, wi) + bi)
    sp = _softplus(-lam)
    log_a = (-LRU_C) * r * sp
    a = jnp.exp(log_a)
    mult2 = jnp.maximum(_one_minus_square(log_a, a), 0.0)
    return r, ig, sp, a, jnp.sqrt(mult2), mult2


SCAN_BLOCKS = 8


def _scans(jobs):
    c = jobs[0][0].shape[1]
    nblk = T // 8
    rows = lax.broadcasted_iota(jnp.int32, (8, c), 0)

    def block(a, b, reverse):
        for s in (1, 2, 4):
            if reverse:
                keep = rows < 8 - s
                a_s = jnp.where(keep, pltpu.roll(a, 8 - s, 0), 1.0)
                b_s = jnp.where(keep, pltpu.roll(b, 8 - s, 0), 0.0)
            else:
                keep = rows >= s
                a_s = jnp.where(keep, pltpu.roll(a, s, 0), 1.0)
                b_s = jnp.where(keep, pltpu.roll(b, s, 0), 0.0)
            b = a * b_s + b
            a = a * a_s
        return a, b

    def step(i, carry):
        out = []
        for (a_ref, b_ref, h_ref, reverse), h_prev in zip(jobs, carry):
            for u in range(SCAN_BLOCKS):
                blk = i * SCAN_BLOCKS + u
                if reverse:
                    blk = nblk - 1 - blk
                t0 = pl.multiple_of(blk * 8, 8)
                a, b = block(a_ref[pl.ds(t0, 8), :], b_ref[pl.ds(t0, 8), :], reverse)
                h = a * h_prev + b
                h_ref[pl.ds(t0, 8), :] = h
                h_prev = jnp.broadcast_to(h[0:1] if reverse else h[7:8], (8, c))
            out.append(h_prev)
        return tuple(out)

    lax.fori_loop(0, nblk // SCAN_BLOCKS, step, tuple(jnp.zeros((8, c), F32) for _ in jobs))


def _rec_specs():
    tok = lambda off: pl.BlockSpec((T, CG), lambda g: (0, g + off))
    per_ch = lambda rows: pl.BlockSpec((rows, CG), lambda g: (0, g))
    wspec = pl.BlockSpec((2, 1, CG, REC_BLOCK), lambda g: (0, g, 0, 0))
    const = lambda shape: pl.BlockSpec(shape, lambda g: (0, 0))
    return tok, per_ch, wspec, const


def _rec_fwd(uy, conv_w, conv_b, w_a, b_a, w_i, b_i, lam):
    tok, per_ch, wspec, const = _rec_specs()

    def body(up_ref, yb_ref, cw_ref, cb_ref, wa_ref, ba_ref, wi_ref, bi_ref, lam_ref, dup_ref, half_ref,
             hf_ref, hb_ref, yrec_ref, am_ref, bx_f, bx_b):
        dup = dup_ref[...]
        same_half = half_ref[...] > 0.5
        taps = _conv_taps(up_ref[...])
        u = cb_ref[...]
        for j in range(4):
            u = u + taps[j] * cw_ref[j:j + 1, :]
        u16 = u.astype(BF16)
        for d, bx_s in enumerate((bx_f, bx_b)):
            wa = _pair_block_diag(wa_ref[d, 0], dup, same_half)
            wi = _pair_block_diag(wi_ref[d, 0], dup, same_half)
            _, ig, _, a, mult, _ = _gates(u, u16, wa, ba_ref[d:d + 1, :], wi, bi_ref[d:d + 1, :],
                                       lam_ref[d:d + 1, :])
            am_ref[2 * d] = a
            am_ref[2 * d + 1] = mult
            bx_s[...] = mult * (ig * u)
        _scans([(am_ref.at[0], bx_f, hf_ref, False), (am_ref.at[2], bx_b, hb_ref, True)])
        gelu, _ = _gelu_and_grad(yb_ref[...])
        yrec_ref[...] = ((hf_ref[...] + hb_ref[...]) * gelu).astype(BF16)

    return pl.pallas_call(
        body, name="rec_fwd",
        out_shape=(jax.ShapeDtypeStruct((T, D_REC), F32), jax.ShapeDtypeStruct((T, D_REC), F32),
                   jax.ShapeDtypeStruct((T, D_REC), BF16), jax.ShapeDtypeStruct((4, T, D_REC), F32)),
        grid=(N_CG,),
        in_specs=[tok(0), tok(N_CG), per_ch(4), per_ch(1), wspec, per_ch(2), wspec, per_ch(2), per_ch(2),
                  const((REC_BLOCK, CG)), const((CG, CG))],
        out_specs=(tok(0), tok(0), tok(0), pl.BlockSpec((4, T, CG), lambda g: (0, 0, g))),
        scratch_shapes=[pltpu.VMEM((T, CG), F32)] * 2,
        compiler_params=_params(dimension_semantics=("parallel",)),
    )(uy, uy, conv_w, conv_b, w_a, b_a, w_i, b_i, lam,
      jnp.asarray(_dup_table(), BF16), jnp.asarray(_pair_mask()))


def _rec_bwd(uy, hf, hb, am, dyrec, conv_w, conv_b, w_a, b_a, w_i, b_i, lam):
    tok, per_ch, wspec, const = _rec_specs()

    def body(up_ref, yb_ref, hf_ref, hb_ref, am_ref, dy_ref, cw_ref, cb_ref, wa_ref, ba_ref, wi_ref, bi_ref,
             lam_ref, dup_ref, dupt_ref, half_ref,
             duy_ref, dcw_ref, dcb_ref, dwa_ref, dba_ref, dwi_ref, dbi_ref, dlam_ref,
             a_s0, a_s1, dh_s, g_s0, g_s1):
        dup = dup_ref[...]
        dup_t = dupt_ref[...]
        same_half = half_ref[...] > 0.5
        taps = _conv_taps(up_ref[...])
        u = cb_ref[...]
        for j in range(4):
            u = u + taps[j] * cw_ref[j:j + 1, :]
        u16 = u.astype(BF16)
        gelu, dgelu = _gelu_and_grad(yb_ref[...])
        dy = dy_ref[...]
        duy_ref[1] = (dy * (hf_ref[...] + hb_ref[...]) * dgelu).astype(BF16)
        dh_s[...] = dy * gelu
        a_s0[...] = _shift_rows(am_ref[0], -1)
        a_s1[...] = _shift_rows(am_ref[2], 1)
        _scans([(a_s0, dh_s, g_s0, True), (a_s1, dh_s, g_s1, False)])
        du = jnp.zeros((T, CG), F32)
        for d, g_s in enumerate((g_s0, g_s1)):
            reverse = d == 1
            wa = _pair_block_diag(wa_ref[d, 0], dup, same_half)
            wi = _pair_block_diag(wi_ref[d, 0], dup, same_half)
            lam_d = lam_ref[d:d + 1, :]
            r = _sigmoid(_dot(u16, wa) + ba_ref[d:d + 1, :])
            ig = _sigmoid(_dot(u16, wi) + bi_ref[d:d + 1, :])
            sp = _softplus(-lam_d)
            a, mult = am_ref[2 * d], am_ref[2 * d + 1]
            mult2 = mult * mult
            g = g_s[...]
            h_prev = _shift_rows(hb_ref[...], -1) if reverse else _shift_rows(hf_ref[...], 1)
            da = g * h_prev
            dmult = g * (ig * u)
            dig = g * mult * u
            du = du + g * mult * ig
            dmult_dlog = jnp.where(mult2 > 0.0, -(a * a) * lax.rsqrt(mult2), 0.0)
            dlog_a = da * a + dmult * dmult_dlog
            dr = dlog_a * ((-LRU_C) * sp)
            dsp = jnp.sum(dlog_a * ((-LRU_C) * r), axis=0, keepdims=True)
            dlam_ref[d:d + 1, :] = dsp * (-_sigmoid(-lam_d))
            dga = dr * r * (1.0 - r)
            dgi = dig * ig * (1.0 - ig)
            dga16 = dga.astype(BF16)
            dgi16 = dgi.astype(BF16)
            du = du + _dot_nt(dga16, wa) + _dot_nt(dgi16, wi)
            dwa_ref[d, 0] = _dot_exact(jnp.where(same_half, _dot_tn(u16, dga16), 0.0), dup_t)
            dwi_ref[d, 0] = _dot_exact(jnp.where(same_half, _dot_tn(u16, dgi16), 0.0), dup_t)
            dba_ref[d:d + 1, :] = jnp.sum(dga, axis=0, keepdims=True)
            dbi_ref[d:d + 1, :] = jnp.sum(dgi, axis=0, keepdims=True)
        dcb_ref[...] = jnp.sum(du, axis=0, keepdims=True)
        for j in range(4):
            dcw_ref[j:j + 1, :] = jnp.sum(du * taps[j], axis=0, keepdims=True)
        dup_in = (_shift_rows(du, -2) * cw_ref[0:1, :] + _shift_rows(du, -1) * cw_ref[1:2, :]
                  + du * cw_ref[2:3, :] + _shift_rows(du, 1) * cw_ref[3:4, :])
        duy_ref[0] = dup_in.astype(BF16)

    wshape = jax.ShapeDtypeStruct((2, N_CG, CG, REC_BLOCK), F32)
    vec = lambda rows: jax.ShapeDtypeStruct((rows, D_REC), F32)
    dup_np = _dup_table()
    return pl.pallas_call(
        body, name="rec_bwd",
        out_shape=(jax.ShapeDtypeStruct((2, T, D_REC), BF16),
                   vec(4), vec(1), wshape, vec(2), wshape, vec(2), vec(2)),
        grid=(N_CG,),
        in_specs=[tok(0), tok(N_CG), tok(0), tok(0), pl.BlockSpec((4, T, CG), lambda g: (0, 0, g)), tok(0),
                  per_ch(4), per_ch(1), wspec, per_ch(2), wspec, per_ch(2), per_ch(2),
                  const((REC_BLOCK, CG)), const((CG, REC_BLOCK)), const((CG, CG))],
        out_specs=(pl.BlockSpec((2, T, CG), lambda g: (0, 0, g)),
                   per_ch(4), per_ch(1), wspec, per_ch(2), wspec, per_ch(2), per_ch(2)),
        scratch_shapes=[pltpu.VMEM((T, CG), F32)] * 5,
        compiler_params=_params(dimension_semantics=("parallel",)),
    )(uy, uy, hf, hb, am, dyrec, conv_w, conv_b, w_a, b_a, w_i, b_i, lam,
      jnp.asarray(dup_np, BF16), jnp.asarray(dup_np.T.copy()), jnp.asarray(_pair_mask()))


TM_MIX = 256


def _mix_specs():
    tok = lambda width, blk=0: pl.BlockSpec((TM_MIX, width), lambda i: (i, blk))
    full = lambda shape: pl.BlockSpec(shape, lambda i: (0, 0))
    return tok, full


def _mix_fwd(x, att, yrec, gg, w_att_o_t, w_rec_o, w_out):
    tok, full = _mix_specs()

    def body(x_ref, att_ref, yr_ref, ga_ref, gr_ref, wao_ref, wro_ref, wo_ref, x1_ref, mixed_ref):
        y_att = _dot_nt(att_ref[...], wao_ref[...])
        y_rec = _dot(yr_ref[...], wro_ref[...])
        mixed = (_sigmoid(ga_ref[...]) * y_att + _sigmoid(gr_ref[...]) * y_rec).astype(BF16)
        mixed_ref[...] = mixed
        x1_ref[...] = x_ref[...] + _dot(mixed, wo_ref[...])

    return pl.pallas_call(
        body, name="mix_fwd",
        out_shape=(jax.ShapeDtypeStruct((T, D), F32), jax.ShapeDtypeStruct((T, D), BF16)),
        grid=(T // TM_MIX,),
        in_specs=[tok(D), tok(D_ATT), tok(D_REC), tok(D, 0), tok(D, 1),
                  full((D, D_ATT)), full((D_REC, D)), full((D, D))],
        out_specs=(tok(D), tok(D)),
        compiler_params=_params(dimension_semantics=("parallel",)),
    )(x, att, yrec, gg, gg, w_att_o_t, w_rec_o, w_out)


def _mix_bwd(dx1, att, yrec, gg, w_att_o_t, w_rec_o, w_out, after):
    tok, full = _mix_specs()

    def body(dx_ref, att_ref, yr_ref, ga_ref, gr_ref, wao_ref, wro_ref, wo_ref, after_ref,
             dgg_ref, dya_ref, dyr_ref, datt_ref, dyrp_ref):
        dmixed = _dot_nt(dx_ref[...].astype(BF16), wo_ref[...])
        y_att = _dot_nt(att_ref[...], wao_ref[...])
        y_rec = _dot(yr_ref[...], wro_ref[...])
        sa = _sigmoid(ga_ref[...])
        sr = _sigmoid(gr_ref[...])
        dgg_ref[0] = (dmixed * y_att * sa * (1.0 - sa)).astype(BF16)
        dgg_ref[1] = (dmixed * y_rec * sr * (1.0 - sr)).astype(BF16)
        dya = (dmixed * sa).astype(BF16)
        dyr = (dmixed * sr).astype(BF16)
        dya_ref[...] = dya
        dyr_ref[...] = dyr
        datt_ref[...] = _dot(dya, wao_ref[...]).astype(BF16)
        dyrp_ref[...] = _dot_nt(dyr, wro_ref[...])

    return pl.pallas_call(
        body, name="mix_bwd",
        out_shape=(jax.ShapeDtypeStruct((2, T, D), BF16),
                   jax.ShapeDtypeStruct((T, D), BF16), jax.ShapeDtypeStruct((T, D), BF16),
                   jax.ShapeDtypeStruct((T, D_ATT), BF16), jax.ShapeDtypeStruct((T, D_REC), F32)),
        grid=(T // TM_MIX,),
        in_specs=[tok(D), tok(D_ATT), tok(D_REC), tok(D, 0), tok(D, 1),
                  full((D, D_ATT)), full((D_REC, D)), full((D, D)), pl.BlockSpec(memory_space=pl.ANY)],
        out_specs=(pl.BlockSpec((2, TM_MIX, D), lambda i: (0, i, 0)),
                   tok(D), tok(D), tok(D_ATT), tok(D_REC)),
        compiler_params=_params(dimension_semantics=("parallel",)),
    )(dx1, att, yrec, gg, gg, w_att_o_t, w_rec_o, w_out, after)


TM_FFN = 256
FF_CHUNK = 1024


def _ffn_loss(x1, target, g2, gf, w_ff1_t, w_ff2):
    n_chunks = D_FF // FF_CHUNK

    def body(x1_ref, tg_ref, g2_ref, gf_ref, w1_hbm, w2_hbm,
             loss_ref, dx1_ref, h2_ref, act_ref, dpre_ref, dx2_ref, dg2_ref, dgf_ref,
             w1, w2, relu_s):
        i = pl.program_id(0)

        @pl.when(i == 0)
        def _():
            pltpu.sync_copy(w1_hbm, w1)
            pltpu.sync_copy(w2_hbm, w2)
            loss_ref[...] = jnp.zeros_like(loss_ref)
            dg2_ref[...] = jnp.zeros_like(dg2_ref)
            dgf_ref[...] = jnp.zeros_like(dgf_ref)

        x1v = x1_ref[...]
        r2 = lax.rsqrt(jnp.mean(x1v * x1v, axis=-1, keepdims=True) + EPS)
        xh2 = x1v * r2
        h2 = (xh2 * g2_ref[...]).astype(BF16)
        h2_ref[...] = h2
        x2 = x1v
        for c in range(n_chunks):
            ff = slice(c * FF_CHUNK, (c + 1) * FF_CHUNK)
            rl = jnp.maximum(_dot_nt(h2, w1[ff, :]), 0.0)
            relu_s[:, ff] = rl
            act = (rl * rl).astype(BF16)
            act_ref[:, ff] = act
            x2 = x2 + _dot(act, w2[ff, :])
        r3 = lax.rsqrt(jnp.mean(x2 * x2, axis=-1, keepdims=True) + EPS)
        xh3 = x2 * r3
        err = xh3 * gf_ref[...] - tg_ref[...]
        loss_ref[...] += 0.5 * jnp.sum(jnp.mean(err * err, axis=-1, keepdims=True))
        dy = err * (1.0 / D)
        dgf_ref[...] += jnp.sum(dy * xh3, axis=0, keepdims=True)
        dx2 = _rms_bwd(dy, xh3, r3, gf_ref[...])
        dx2_16 = dx2.astype(BF16)
        dx2_ref[...] = dx2_16
        dh2 = jnp.zeros((TM_FFN, D), F32)
        for c in range(n_chunks):
            ff = slice(c * FF_CHUNK, (c + 1) * FF_CHUNK)
            dpre = (_dot_nt(dx2_16, w2[ff, :]) * (2.0 * relu_s[:, ff])).astype(BF16)
            dpre_ref[:, ff] = dpre
            dh2 = dh2 + _dot(dpre, w1[ff, :])
        dg2_ref[...] += jnp.sum(dh2 * xh2, axis=0, keepdims=True)
        dx1_ref[...] = dx2 + _rms_bwd(dh2, xh2, r2, g2_ref[...])

    tok = lambda width: pl.BlockSpec((TM_FFN, width), lambda i: (i, 0))
    vec = pl.BlockSpec((1, D), lambda i: (0, 0))
    hbm = pl.BlockSpec(memory_space=pl.ANY)
    return pl.pallas_call(
        body, name="ffn_loss",
        out_shape=(jax.ShapeDtypeStruct((8, 128), F32), jax.ShapeDtypeStruct((T, D), F32),
                   jax.ShapeDtypeStruct((T, D), BF16), jax.ShapeDtypeStruct((T, D_FF), BF16),
                   jax.ShapeDtypeStruct((T, D_FF), BF16), jax.ShapeDtypeStruct((T, D), BF16),
                   jax.ShapeDtypeStruct((1, D), F32), jax.ShapeDtypeStruct((1, D), F32)),
        grid=(T // TM_FFN,),
        in_specs=[tok(D), tok(D), vec, vec, hbm, hbm],
        out_specs=(pl.BlockSpec((8, 128), lambda i: (0, 0)), tok(D), tok(D), tok(D_FF), tok(D_FF), tok(D),
                   vec, vec),
        scratch_shapes=[pltpu.VMEM((D_FF, D), BF16), pltpu.VMEM((D_FF, D), BF16),
                        pltpu.VMEM((TM_FFN, D_FF), F32)],
        compiler_params=_params(dimension_semantics=("arbitrary",)),
    )(x1, target, g2, gf, w_ff1_t, w_ff2)


def _local_step(x, target, p, late_weights, reduce_first, reduce_early, reduce_late):
    bias = _rpb_rows(p["rpb"])
    pairs = lambda w: w.reshape(2, N_CG, CG, REC_BLOCK)
    w_a, w_i = pairs(p["w_rg_a"]), pairs(p["w_rg_i"])
    rec_params = (p["conv_w"], p["conv_b"], w_a, p["b_rg_a"], w_i, p["b_rg_i"], p["lru_lambda"])

    qkv, uy, gg, h = _in_proj(x, p["ln1_g"], p["w_in_t"], p["b_in"])
    att = _att_fwd(qkv, bias)
    hf, hb, yrec, am = _rec_fwd(uy, *rec_params)
    p = {**p, **late_weights(yrec, 0)}
    x1, mixed = _mix_fwd(x, att, yrec, gg, p["w_att_o_t"], p["w_rec_o"], p["w_out"])
    p = {**p, **late_weights(x1, 1)}
    loss8, dx1, h2, act, dpre, dx2, g_ln2, g_lnf = _ffn_loss(
        x1, target, p["ln2_g"], p["lnf_g"], p["w_ff1_t"], p["w_ff2"])

    grads = {"ln2_g": g_ln2, "lnf_g": g_lnf,
             "w_ff1_t": _matmul(dpre, h2, "tn", BF16, "g_w_ff1"),
             "w_ff2": _matmul(act, dx2, "tn", BF16, "g_w_ff2")}
    dgg, dya, dyr, datt, dyrp = _mix_bwd(dx1, att, yrec, gg, p["w_att_o_t"], p["w_rec_o"], p["w_out"],
                                         reduce_first(grads, None))
    lam_after = rec_params[-1] + reduce_first(None, dgg)[0, 0]
    duy, g_cw, g_cb, g_wa, g_ba, g_wi, g_bi, g_lam = _rec_bwd(uy, hf, hb, am, dyrp, *rec_params[:-1], lam_after)
    blocks = lambda g: g.reshape(2, N_REC_BLOCKS, REC_BLOCK, REC_BLOCK)
    grads.update({
        "w_att_o_t": _matmul(dya, att, "tn", BF16, "g_w_att_o"),
        "conv_w": g_cw, "conv_b": g_cb, "w_rg_a": blocks(g_wa), "b_rg_a": g_ba,
        "w_rg_i": blocks(g_wi), "b_rg_i": g_bi, "lru_lambda": g_lam,
        "w_rec_o": _matmul(yrec, dyr, "tn", BF16, "g_w_rec_o"),
        "w_out": _matmul(mixed, dx1, "tn", BF16, "g_w_out"),
    })
    dqkv, gbias = _att_bwd(qkv, bias, datt, reduce_early(grads))
    dz = (dqkv, duy, dgg)
    g_w_in_t, g_b_in = _grad_w_in(dz, h)
    grads.update(w_in_t=g_w_in_t, b_in=g_b_in)
    grad_x, g_ln1 = _dh_norm1_bwd(dz, p["w_in_t"], x, p["ln1_g"], dx1, reduce_late(grads))
    grads.update(ln1_g=g_ln1, rpb=_rpb_fold(gbias))
    return loss8[0:1, 0:1], grad_x, grads


MESH_ID = pl.DeviceIdType.MESH
ANY = pl.BlockSpec(memory_space=pl.ANY)

CHAN_BLOCK_ROWS = 32
GATE_ROWS = 2 * 2 * N_REC_BLOCKS * REC_BLOCK * REC_BLOCK // (N_DEV * D)
SECTIONS = (("w_in_t", 704, D), ("w_rec_o", 128, D), ("w_out", 128, D), ("w_ff1_t", 512, D),
            ("w_ff2", 512, D), ("chan", CHAN_BLOCK_ROWS, D), ("w_att_o_t", 128, D_ATT),
            ("gates", GATE_ROWS, D))
N_SEC = len(SECTIONS)
N_CHAN_ROWS = 10
CHAN = (("conv_w", 4), ("b_rg_a", 2), ("b_rg_i", 2), ("lru_lambda", 2))


def _position():
    return lax.axis_index("x"), lax.axis_index("y"), lax.axis_index("c")


def _other_chips(x, y):
    return [(1 - x, y), (x, 1 - y), (1 - x, 1 - y)]


PASS_ON_IDS, PAIR_EARLY_ID, PAIR_LATE_ID, PAIR_FIRST_ID, SMALL_PASS_ON_ID = (1, 4), 2, 3, 5, 6


def _pair_handshake(x, y, c):
    barrier = pltpu.get_barrier_semaphore()
    pl.semaphore_signal(barrier, inc=1, device_id=(x, y, 1 - c), device_id_type=MESH_ID)
    pl.semaphore_wait(barrier, 1)


def _block_of(ref, dev, rows):
    return ref.at[pl.ds(pl.multiple_of(dev * rows, 16), rows)]


def _all_gather(shards, name):
    ns = len(shards)

    def body(*refs):
        x_refs, out_refs, done_ref = refs[:ns], refs[ns:2 * ns], refs[2 * ns]
        send_sems, recv_sems, local_sems = refs[2 * ns + 1:]
        done_ref[0, 0] = 0.0
        x, y, c = _position()
        me, sibling = (x, y, c), (x, y, 1 - c)
        x_nbr, y_nbr, diagonal = _other_chips(x, y)
        north = c == 1
        relay_from = (jnp.where(north, x_nbr[0], y_nbr[0]), jnp.where(north, x_nbr[1], y_nbr[1]))
        relay_to = (jnp.where(north, y_nbr[0], x_nbr[0]), jnp.where(north, y_nbr[1], x_nbr[1]))

        def rows(s, px, py, pc):
            return _block_of(out_refs[s], 4 * px + 2 * py + pc, shards[s].shape[0])

        def copy(k, s, block, to, from_shard=False):
            return pltpu.make_async_remote_copy(
                src_ref=x_refs[s] if from_shard else rows(s, *block), dst_ref=rows(s, *block),
                send_sem=send_sems.at[k * ns + s], recv_sem=recv_sems.at[k * ns + s],
                device_id=to, device_id_type=MESH_ID)

        sections = range(ns)
        mine = [pltpu.make_async_copy(x_refs[s], rows(s, *me), local_sems.at[s]) for s in sections]
        sent = [copy(k, s, me, to, True) for k, to in enumerate((sibling, (*x_nbr, c), (*y_nbr, c)))
                for s in sections]
        for cp in mine + sent:
            cp.start()
        for s in sections:
            copy(1, s, (*x_nbr, c), me).wait_recv()
            copy(2, s, (*y_nbr, c), me).wait_recv()
            sent += [copy(3, s, (*relay_from, c), (*relay_to, c)),
                     copy(4, s, (*x_nbr, c), sibling), copy(5, s, (*y_nbr, c), sibling)]
            for cp in sent[-3:]:
                cp.start()
        for s in sections:
            copy(3, s, (*diagonal, c), me).wait_recv()
            sent.append(copy(6, s, (*diagonal, c), sibling))
            sent[-1].start()
        for s in sections:
            copy(0, s, sibling, me).wait_recv()
            for k, chip in ((4, x_nbr), (5, y_nbr), (6, diagonal)):
                copy(k, s, (*chip, 1 - c), me).wait_recv()
        for cp in sent:
            cp.wait_send()
        for cp in mine:
            cp.wait()

    return pl.pallas_call(
        body, name=name,
        out_shape=tuple(jax.ShapeDtypeStruct((N_DEV * s.shape[0], s.shape[1]), s.dtype) for s in shards)
        + (jax.ShapeDtypeStruct((1, 1), F32),),
        in_specs=[ANY] * ns,
        out_specs=(ANY,) * ns + (pl.BlockSpec(memory_space=pltpu.SMEM),),
        scratch_shapes=[pltpu.SemaphoreType.DMA((7 * ns,)), pltpu.SemaphoreType.DMA((7 * ns,)),
                        pltpu.SemaphoreType.DMA((ns,))],
    )(*shards)


HBM = pl.BlockSpec(memory_space=pltpu.HBM)
SEM = pl.BlockSpec(memory_space=pltpu.SEMAPHORE)
EFFECT = pltpu.SideEffectType.DATAFLOW_SIDE_EFFECTING


def _in_hbm(a):
    return pltpu.with_memory_space_constraint(a, pltpu.HBM)


def _first_hop_copies(shards, x_refs, zones, send_sems, recv_sems):
    ns = len(shards)
    x, y, c = _position()
    targets = [(x, y, 1 - c)] + [(cx, cy, c) for cx, cy in _other_chips(x, y)]
    return [pltpu.make_async_remote_copy(
        src_ref=x_refs[s], dst_ref=_block_of(zones[s], 4 * x + 2 * y + c, shards[s].shape[0]),
        send_sem=send_sems.at[k * ns + s], recv_sem=recv_sems.at[k * ns + s],
        device_id=to, device_id_type=MESH_ID)
        for k, to in enumerate(targets) for s in range(ns)]


def _after_all(arrays, name):
    def body(*refs):
        refs[-1][...] = jnp.zeros_like(refs[-1])

    return pl.pallas_call(
        body, name=name,
        out_shape=jax.ShapeDtypeStruct((8, LANES), F32),
        in_specs=[pl.BlockSpec(memory_space=pl.ANY)] * len(arrays),
        out_specs=pl.BlockSpec(memory_space=pltpu.VMEM),
    )(*arrays)


def _own_blocks_placed(shards, after, name):
    ns = len(shards)
    x, y, c = _position()
    me = jnp.reshape(4 * x + 2 * y + c, (1,)).astype(jnp.int32)
    if after is not None:
        shards = [*shards[:-1], shards[-1] + after.astype(shards[-1].dtype)]

    def body(me_ref, *refs):
        for s in range(ns):
            refs[ns + s][...] = refs[s][...]

    return pl.pallas_call(
        body, name=name,
        out_shape=tuple(jax.ShapeDtypeStruct((N_DEV * s.shape[0], s.shape[1]), s.dtype) for s in shards),
        grid_spec=pltpu.PrefetchScalarGridSpec(
            num_scalar_prefetch=1, grid=(1,),
            in_specs=[pl.BlockSpec(s.shape, lambda i, me: (0, 0)) for s in shards],
            out_specs=tuple(pl.BlockSpec(s.shape, lambda i, me: (me[0], 0)) for s in shards)),
        compiler_params=_params(dimension_semantics=("arbitrary",)),
    )(me, *shards)


def _gather_start(shards, after, name):
    ns = len(shards)
    zones = _own_blocks_placed(shards, after, name + "_own_blocks")

    def body(*refs):
        for cp in _first_hop_copies(shards, refs[:ns], refs[ns:2 * ns], refs[2 * ns], refs[2 * ns + 1]):
            cp.start()
        refs[-1][...] = jnp.zeros_like(refs[-1])

    out = pl.pallas_call(
        body, name=name,
        out_shape=(pltpu.SemaphoreType.DMA((4 * ns,)), pltpu.SemaphoreType.DMA((4 * ns,)),
                   *[pltpu.HBM(a.shape, a.dtype) for a in (*shards, *zones)],
                   jax.ShapeDtypeStruct((8, LANES), F32)),
        in_specs=[HBM] * (2 * ns),
        out_specs=(SEM, SEM, *[HBM] * (2 * ns), pl.BlockSpec(memory_space=pltpu.VMEM)),
        input_output_aliases={i: 2 + i for i in range(2 * ns)},
        compiler_params=pltpu.CompilerParams(has_side_effects=EFFECT),
    )(*[_in_hbm(a) for a in shards], *[_in_hbm(a) for a in zones])
    return out[0], out[1], out[2:2 + ns], out[2 + ns:2 + 2 * ns], out[-1]


def _gather_wait(send_sems, recv_sems, shards, zones, which, after, name):
    ns = len(shards)

    def body(*refs):
        copies = _first_hop_copies(shards, refs[:ns], refs[ns:2 * ns], refs[2 * ns], refs[2 * ns + 1])
        for i, cp in enumerate(copies):
            if i % ns in which:
                cp.wait_send()
                cp.wait_recv()

    out = pl.pallas_call(
        body, name=name,
        out_shape=tuple(pltpu.HBM(a.shape, a.dtype) for a in (*shards, *zones)),
        in_specs=[HBM] * (2 * ns) + [SEM, SEM, ANY],
        out_specs=(HBM,) * (2 * ns),
        input_output_aliases={i: i for i in range(2 * ns)},
        compiler_params=pltpu.CompilerParams(has_side_effects=EFFECT),
    )(*shards, *zones, send_sems, recv_sems, after)
    return out[:ns], out[ns:]


def _gather_pass_on(rows, zones, barrier_id, name):
    ns = len(zones)

    def body(*refs):
        in_refs, out_refs = refs[:ns], refs[ns:2 * ns]
        send_sems, recv_sems = refs[2 * ns:]
        x, y, c = _position()
        _pair_handshake(x, y, c)
        copies = [pltpu.make_async_remote_copy(
            src_ref=_block_of(in_refs[s], 4 * cx + 2 * cy + c, rows[s]),
            dst_ref=_block_of(out_refs[s], 4 * cx + 2 * cy + c, rows[s]),
            send_sem=send_sems.at[j * ns + s], recv_sem=recv_sems.at[j * ns + s],
            device_id=(x, y, 1 - c), device_id_type=MESH_ID)
            for j, (cx, cy) in enumerate(_other_chips(x, y)) for s in range(ns)]
        for cp in copies:
            cp.start()
        for cp in copies:
            cp.wait_recv()
        for cp in copies:
            cp.wait_send()

    return pl.pallas_call(
        body, name=name,
        out_shape=tuple(jax.ShapeDtypeStruct(z.shape, z.dtype) for z in zones),
        in_specs=[ANY] * ns, out_specs=(ANY,) * ns,
        input_output_aliases={i: i for i in range(ns)},
        scratch_shapes=[pltpu.SemaphoreType.DMA((3 * ns,)), pltpu.SemaphoreType.DMA((3 * ns,))],
        compiler_params=pltpu.CompilerParams(collective_id=barrier_id),
    )(*zones)


def _pair_copies(sections, g_refs, land, send_sems, recv_sems):
    ns = len(sections)
    x, y, c = _position()
    return [pltpu.make_async_remote_copy(
        src_ref=_block_of(g_refs[s], 2 * k + 1 - c, rows), dst_ref=land[s].at[k],
        send_sem=send_sems.at[k * ns + s], recv_sem=recv_sems.at[k * ns + s],
        device_id=(x, y, 1 - c), device_id_type=MESH_ID)
        for k in range(N_CHIPS) for s, (_, rows, _) in enumerate(sections)]


def _pair_exchange_start(sections, grads, barrier_id, name):
    ns = len(sections)

    def body(*refs):
        _pair_handshake(*_position())
        for cp in _pair_copies(sections, refs[:ns], refs[ns:2 * ns], refs[2 * ns], refs[2 * ns + 1]):
            cp.start()
        refs[-1][...] = jnp.zeros_like(refs[-1])

    zones = [lax.empty((N_CHIPS, rows, cols), BF16) for _, rows, cols in sections]
    n = N_CHIPS * ns
    out = pl.pallas_call(
        body, name=name,
        out_shape=(pltpu.SemaphoreType.DMA((n,)), pltpu.SemaphoreType.DMA((n,)),
                   *[pltpu.HBM(a.shape, a.dtype) for a in (*grads, *zones)],
                   jax.ShapeDtypeStruct((8, LANES), F32)),
        in_specs=[HBM] * (2 * ns),
        out_specs=(SEM, SEM, *[HBM] * (2 * ns), pl.BlockSpec(memory_space=pltpu.VMEM)),
        input_output_aliases={i: 2 + i for i in range(2 * ns)},
        compiler_params=pltpu.CompilerParams(has_side_effects=EFFECT, collective_id=barrier_id),
    )(*[_in_hbm(a) for a in grads], *[_in_hbm(a) for a in zones])
    return out[0], out[1], out[2:2 + ns], out[2 + ns:2 + 2 * ns], out[-1]


def _pair_exchange_wait(sections, send_sems, recv_sems, grads, zones, after, name):
    ns = len(sections)

    def body(*refs):
        for cp in _pair_copies(sections, refs[:ns], refs[ns:2 * ns], refs[2 * ns], refs[2 * ns + 1]):
            cp.wait_send()
            cp.wait_recv()

    out = pl.pallas_call(
        body, name=name,
        out_shape=tuple(pltpu.HBM(a.shape, a.dtype) for a in (*grads, *zones)),
        in_specs=[HBM] * (2 * ns) + [SEM, SEM, ANY],
        out_specs=(HBM,) * (2 * ns),
        input_output_aliases={i: i for i in range(2 * ns)},
        compiler_params=pltpu.CompilerParams(has_side_effects=EFFECT),
    )(*grads, *zones, send_sems, recv_sems, after)
    return out[:ns], out[ns:]


def _pair_add(sections, grads, got, core, name):
    ns = len(sections)

    def body(core_ref, *refs):
        g_refs, got_refs, p_refs = refs[:ns], refs[ns:2 * ns], refs[2 * ns:]
        for s in range(ns):
            p_refs[s][0] = (g_refs[s][...].astype(F32) + got_refs[s][0].astype(F32)).astype(BF16)

    slot = [pl.BlockSpec((1, rows, cols), lambda k, c: (k, 0, 0)) for _, rows, cols in sections]
    return pl.pallas_call(
        body, name=name,
        out_shape=tuple(jax.ShapeDtypeStruct((N_CHIPS, rows, cols), BF16) for _, rows, cols in sections),
        grid_spec=pltpu.PrefetchScalarGridSpec(
            num_scalar_prefetch=1, grid=(N_CHIPS,),
            in_specs=[pl.BlockSpec((rows, cols), lambda k, c: (2 * k + c[0], 0)) for _, rows, cols in sections]
            + slot,
            out_specs=tuple(slot)),
        compiler_params=_params(dimension_semantics=("parallel",)),
    )(core, *grads, *got)


def _chip_copies(sections, p_refs, land, send_sems, recv_sems):
    ns = len(sections)
    x, y, c = _position()
    return [pltpu.make_async_remote_copy(
        src_ref=p_refs[s].at[2 * cx + cy], dst_ref=land[s].at[j],
        send_sem=send_sems.at[j * ns + s], recv_sem=recv_sems.at[j * ns + s],
        device_id=(cx, cy, c), device_id_type=MESH_ID)
        for j, (cx, cy) in enumerate(_other_chips(x, y)) for s in range(ns)]


def _chip_exchange(sections, parts, name):
    ns = len(sections)

    def body(*refs):
        copies = _chip_copies(sections, refs[:ns], refs[ns:2 * ns], *refs[2 * ns:])
        for cp in copies:
            cp.start()
        for cp in copies:
            cp.wait_recv()
        for cp in copies:
            cp.wait_send()

    n = 3 * ns
    return pl.pallas_call(
        body, name=name,
        out_shape=tuple(jax.ShapeDtypeStruct((3, rows, cols), BF16) for _, rows, cols in sections),
        in_specs=[ANY] * ns, out_specs=(ANY,) * ns,
        scratch_shapes=[pltpu.SemaphoreType.DMA((n,)), pltpu.SemaphoreType.DMA((n,))],
    )(*parts)


def _chip_exchange_start(sections, parts, name):
    ns = len(sections)

    def body(*refs):
        p_refs, land = refs[:ns], refs[ns:2 * ns]
        send_sems, recv_sems = refs[2 * ns], refs[2 * ns + 1]
        token = refs[-1]
        for cp in _chip_copies(sections, p_refs, land, send_sems, recv_sems):
            cp.start()
        token[...] = jnp.zeros_like(token)

    zones = [lax.empty((3, rows, cols), BF16) for _, rows, cols in sections]
    out = pl.pallas_call(
        body, name=name,
        out_shape=(pltpu.SemaphoreType.DMA((3 * ns,)), pltpu.SemaphoreType.DMA((3 * ns,)),
                   *[pltpu.HBM(a.shape, a.dtype) for a in parts], *[pltpu.HBM(a.shape, a.dtype) for a in zones],
                   jax.ShapeDtypeStruct((8, LANES), F32)),
        in_specs=[HBM] * (2 * ns),
        out_specs=(SEM, SEM, *[HBM] * (2 * ns), pl.BlockSpec(memory_space=pltpu.VMEM)),
        input_output_aliases={i: 2 + i for i in range(2 * ns)},
        compiler_params=pltpu.CompilerParams(has_side_effects=EFFECT),
    )(*[_in_hbm(a) for a in parts], *[_in_hbm(a) for a in zones])
    return out[0], out[1], out[2:2 + ns], out[2 + ns:2 + 2 * ns], out[-1]


def _chip_exchange_wait(sections, send_sems, recv_sems, parts, zones, after, name):
    ns = len(sections)

    def body(*refs):
        p_refs, land = refs[:ns], refs[ns:2 * ns]
        for cp in _chip_copies(sections, p_refs, land, refs[2 * ns], refs[2 * ns + 1]):
            cp.wait_send()
            cp.wait_recv()

    out = pl.pallas_call(
        body, name=name,
        out_shape=tuple(pltpu.HBM(a.shape, a.dtype) for a in (*parts, *zones)),
        in_specs=[HBM] * (2 * ns) + [SEM, SEM, ANY],
        out_specs=(HBM,) * (2 * ns),
        input_output_aliases={i: i for i in range(2 * ns)},
        compiler_params=pltpu.CompilerParams(has_side_effects=EFFECT),
    )(*parts, *zones, send_sems, recv_sems, after)
    return out[:ns], out[ns:]


def _grad_finish(sections, parts, far, chip, name):
    ns = len(sections)

    def body(chip_ref, *refs):
        p_refs, b_refs, g_refs = refs[:ns], refs[ns:2 * ns], refs[2 * ns:]
        for s in range(ns):
            g = p_refs[s][0].astype(F32)
            for j in range(3):
                g = g + b_refs[s][j].astype(F32)
            g_refs[s][...] = g

    half = [(rows // 2, cols) for _, rows, cols in sections]
    return pl.pallas_call(
        body, name=name,
        out_shape=tuple(jax.ShapeDtypeStruct((rows, cols), F32) for _, rows, cols in sections),
        grid_spec=pltpu.PrefetchScalarGridSpec(
            num_scalar_prefetch=1, grid=(2,),
            in_specs=[pl.BlockSpec((1, r, c), lambda i, chip: (chip[0], i, 0)) for r, c in half]
            + [pl.BlockSpec((3, r, c), lambda i, chip: (0, i, 0)) for r, c in half],
            out_specs=tuple(pl.BlockSpec((r, c), lambda i, chip: (i, 0)) for r, c in half)),
        compiler_params=_params(dimension_semantics=("parallel",)),
    )(chip, *parts, *far)


def _sum_devices(parts, rows, name):
    cols = parts.shape[1]
    tr = rows // 2

    def body(*refs):
        s = refs[0][...].astype(F32)
        for d in range(1, N_DEV):
            s = s + refs[d][...].astype(F32)
        refs[N_DEV][...] = s

    return pl.pallas_call(
        body, name=name,
        out_shape=jax.ShapeDtypeStruct((rows, cols), F32),
        grid=(2,),
        in_specs=[pl.BlockSpec((tr, cols), lambda i, d=d: (2 * d + i, 0)) for d in range(N_DEV)],
        out_specs=pl.BlockSpec((tr, cols), lambda i: (i, 0)),
        compiler_params=_params(dimension_semantics=("parallel",)),
    )(*([parts] * N_DEV))


def _adamw_step(w_ref, g_ref, m_ref, v_ref, d_ref, nm_ref, nv_ref):
    c1 = 1.0 / (1.0 - ADAM_B1 ** ADAM_STEP)
    c2 = 1.0 / (1.0 - ADAM_B2 ** ADAM_STEP)
    gv = g_ref[...]
    nm = ADAM_B1 * m_ref[...] + (1.0 - ADAM_B1) * gv
    nv = ADAM_B2 * v_ref[...] + (1.0 - ADAM_B2) * (gv * gv)
    nm_ref[...] = nm
    nv_ref[...] = nv
    d_ref[...] = (-ADAM_LR) * ((nm * c1) / (jnp.sqrt(nv * c2) + ADAM_EPS) + ADAM_WD * w_ref[...])


def _adamw_small(params, name):
    n = len(params)

    def body(*refs):
        for k in range(n):
            _adamw_step(*refs[4 * k:4 * k + 4], *refs[4 * n + 3 * k:4 * n + 3 * k + 3])

    out = pl.pallas_call(
        body, name=name,
        out_shape=tuple(jax.ShapeDtypeStruct(p[0].shape, F32) for p in params for _ in range(3)),
    )(*[a for p in params for a in p])
    return [out[3 * k:3 * k + 3] for k in range(n)]


def _adamw(w, g, m, v, name):
    rows, cols = w.shape
    tr = rows
    while tr * cols * 4 > (1 << 20) and tr % 16 == 0:
        tr //= 2

    def body(*refs):
        _adamw_step(*refs)

    spec = pl.BlockSpec((tr, cols), lambda i: (i, 0))
    shape = jax.ShapeDtypeStruct((rows, cols), F32)
    return pl.pallas_call(
        body, name=name,
        out_shape=(shape, shape, shape),
        grid=(rows // tr,),
        in_specs=[spec] * 4, out_specs=(spec,) * 3,
        compiler_params=_params(dimension_semantics=("parallel",)),
    )(w, g, m, v)


NAMES = ("ln1_g", "w_in", "b_in", "rpb", "w_att_o", "conv_w", "conv_b", "w_rg_a", "b_rg_a", "w_rg_i",
         "b_rg_i", "lru_lambda", "w_rec_o", "w_out", "ln2_g", "w_ff1", "w_ff2", "lnf_g")
TRANSPOSED = {"w_in": "w_in_t", "w_att_o": "w_att_o_t", "w_ff1": "w_ff1_t"}
ROW_SHARDED = ("w_rec_o", "w_out", "w_ff2")
REPLICATED = (("ln1_g", (1, D)), ("b_in", (1, D_IN)), ("rpb", (N_HEADS * N_RPB_R, N_RPB_C)),
              ("conv_b", (1, D_REC)), ("w_rg_a", (2 * N_REC_BLOCKS * REC_BLOCK, REC_BLOCK)),
              ("w_rg_i", (2 * N_REC_BLOCKS * REC_BLOCK, REC_BLOCK)), ("ln2_g", (1, D)), ("lnf_g", (1, D)))
GATE_BLOCKS = ("w_rg_a", "w_rg_i")
SMALL_ROWS = 112


def _chan_bits(vectors):
    chan = jnp.concatenate(vectors, axis=0)
    bits = lax.bitcast_convert_type(chan, BF16).reshape(-1)
    return jnp.pad(bits, (0, CHAN_BLOCK_ROWS * D - bits.shape[0])).reshape(CHAN_BLOCK_ROWS, D)


def _chan_from_bits(gathered):
    bits = gathered.reshape(N_DEV, CHAN_BLOCK_ROWS * D)[:, :2 * N_CHAN_ROWS * LANES]
    chan = lax.bitcast_convert_type(bits.reshape(N_DEV, N_CHAN_ROWS, LANES, 2), F32)
    return chan.transpose(1, 0, 2).reshape(N_CHAN_ROWS, D)


def kernel(x, ln1_g, w_in, b_in, rpb, w_att_o, conv_w, conv_b, w_rg_a, b_rg_a, w_rg_i, b_rg_i, lru_lambda, w_rec_o, w_out, ln2_g, w_ff1, w_ff2, lnf_g, loss_target, m_ln1_g, m_w_in, m_b_in, m_rpb, m_w_att_o, m_conv_w, m_conv_b, m_w_rg_a, m_b_rg_a, m_w_rg_i, m_b_rg_i, m_lru_lambda, m_w_rec_o, m_w_out, m_ln2_g, m_w_ff1, m_w_ff2, m_lnf_g, v_ln1_g, v_w_in, v_b_in, v_rpb, v_w_att_o, v_conv_w, v_conv_b, v_w_rg_a, v_b_rg_a, v_w_rg_i, v_b_rg_i, v_lru_lambda, v_w_rec_o, v_w_out, v_ln2_g, v_w_ff1, v_w_ff2, v_lnf_g):
    w = dict(zip(NAMES, (ln1_g, w_in, b_in, rpb, w_att_o, conv_w, conv_b, w_rg_a, b_rg_a, w_rg_i,
                         b_rg_i, lru_lambda, w_rec_o, w_out, ln2_g, w_ff1, w_ff2, lnf_g)))
    m = dict(zip(NAMES, (m_ln1_g, m_w_in, m_b_in, m_rpb, m_w_att_o, m_conv_w, m_conv_b, m_w_rg_a,
                         m_b_rg_a, m_w_rg_i, m_b_rg_i, m_lru_lambda, m_w_rec_o, m_w_out, m_ln2_g,
                         m_w_ff1, m_w_ff2, m_lnf_g)))
    v = dict(zip(NAMES, (v_ln1_g, v_w_in, v_b_in, v_rpb, v_w_att_o, v_conv_w, v_conv_b, v_w_rg_a,
                         v_b_rg_a, v_w_rg_i, v_b_rg_i, v_lru_lambda, v_w_rec_o, v_w_out, v_ln2_g,
                         v_w_ff1, v_w_ff2, v_lnf_g)))
    xi, yi, ci = _position()

    shard = {t: w[n][0].T.astype(BF16) for n, t in TRANSPOSED.items()}
    shard.update({n: w[n][0].astype(BF16) for n in ROW_SHARDED})
    shard["chan"] = _chan_bits([w[n][0] for n, _ in CHAN])
    first, later = ("w_in_t", "chan"), ("w_rec_o", "w_out", "w_att_o_t", "w_ff1_t", "w_ff2")
    *gathered, done = _all_gather([shard[n] for n in first], "weight_all_gather")
    p = dict(zip(first, gathered))
    send_sems, recv_sems, sent, zones, token = _gather_start([shard[n] for n in later], done,
                                                             "weight_gather_start")

    travelling = {"shards": sent, "zones": zones}
    stages = (("w_rec_o", "w_out", "w_att_o_t"), ("w_ff1_t", "w_ff2"))

    def late_weights(after, stage):
        which = [later.index(n) for n in stages[stage]]
        travelling["shards"], travelling["zones"] = _gather_wait(
            send_sems, recv_sems, travelling["shards"], travelling["zones"], which, after,
            "weight_gather_wait_%d" % stage)
        return dict(zip(stages[stage], _gather_pass_on(
            [shard[n].shape[0] for n in stages[stage]], [travelling["zones"][i] for i in which],
            PASS_ON_IDS[stage], "weight_gather_pass_on_%d" % stage)))

    chan = _chan_from_bits(p.pop("chan"))
    r0 = 0
    for n, rows in CHAN:
        p[n] = chan[r0:r0 + rows]
        r0 += rows
    p.update(ln1_g=w["ln1_g"], b_in=w["b_in"] + token[0, 0], rpb=w["rpb"][0], conv_b=w["conv_b"],
             w_rg_a=w["w_rg_a"][0], w_rg_i=w["w_rg_i"][0], ln2_g=w["ln2_g"],
             lnf_g=w["lnf_g"].reshape(1, D))

    core = jnp.reshape(ci, (1,)).astype(jnp.int32)
    chip = jnp.reshape(2 * xi + yi, (1,)).astype(jnp.int32)
    first_sections = tuple(s for s in SECTIONS if s[0] in ("w_ff1_t", "w_ff2"))
    late_sections = SECTIONS[:1]
    early_sections = tuple(s for s in SECTIONS[1:] if s not in first_sections)
    in_flight = {}

    def pair_sum_and_send(group, sections, after):
        send_sems, recv_sems, sect, zones, _ = in_flight["pair_" + group]
        sect, got = _pair_exchange_wait(sections, send_sems, recv_sems, sect, zones, after,
                                        "grad_pair_exchange_wait_" + group)
        parts = _pair_add(sections, sect, got, core, "grad_pair_add_" + group)
        in_flight[group] = _chip_exchange_start(sections, parts, "grad_chip_exchange_start_" + group)
        return in_flight[group][-1]

    def pair_exchange_at_once(group, sections, grads, barrier_id):
        in_flight["pair_" + group] = _pair_exchange_start(
            sections, [grads[n] for n, _, _ in sections], barrier_id, "grad_pair_exchange_start_" + group)
        return pair_sum_and_send(group, sections, in_flight["pair_" + group][-1])

    def reduce_first(grads, after):
        if grads is None:
            return pair_sum_and_send("first", first_sections, after)
        in_flight["pair_first"] = _pair_exchange_start(
            first_sections, [grads[n] for n, _, _ in first_sections], PAIR_FIRST_ID,
            "grad_pair_exchange_start_first")
        return in_flight["pair_first"][-1]

    def reduce_early(grads):
        chan_g = jnp.concatenate([grads[n] for n, _ in CHAN], axis=0)
        chan_g = chan_g.reshape(N_CHAN_ROWS, N_DEV, LANES).transpose(1, 0, 2).astype(BF16)
        chan_g = jnp.pad(chan_g.reshape(N_DEV, -1), ((0, 0), (0, CHAN_BLOCK_ROWS * D - N_CHAN_ROWS * LANES)))
        grads["chan"] = chan_g.reshape(N_DEV * CHAN_BLOCK_ROWS, D)
        grads["gates"] = jnp.concatenate([grads[n].reshape(-1, D) for n in GATE_BLOCKS], axis=0).astype(BF16)
        return pair_exchange_at_once("early", early_sections, grads, PAIR_EARLY_ID)[0, 0]

    def finish(group, sections, after, name):
        send_sems, recv_sems, parts, zones, _ = in_flight[group]
        parts, far = _chip_exchange_wait(sections, send_sems, recv_sems, parts, zones, after,
                                         "grad_chip_exchange_wait_" + name)
        return dict(zip((n for n, _, _ in sections),
                        _grad_finish(sections, parts, far, chip, "grad_finish_" + name)))

    summed = {}

    def reduce_late(grads):
        in_flight["pair_late"] = _pair_exchange_start(
            late_sections, [grads[n] for n, _, _ in late_sections], PAIR_LATE_ID,
            "grad_pair_exchange_start_late")
        summed.update(finish("first", first_sections, in_flight["pair_late"][-1], "first"))
        summed.update(finish("early", early_sections, summed["w_ff2"], "early"))
        return pair_sum_and_send("late", late_sections, summed["gates"])

    loss_part, grad_x, grads = _local_step(x[0], loss_target[0], p, late_weights, reduce_first, reduce_early,
                                           reduce_late)

    flat = jnp.concatenate([grads[n].reshape(-1) for n, _ in REPLICATED if n not in GATE_BLOCKS]
                           + [loss_part.reshape(-1)])
    n_small = flat.shape[0]
    flat = jnp.pad(flat, (0, SMALL_ROWS * LANES - n_small)).reshape(SMALL_ROWS, LANES)
    send_sems, recv_sems, sent, zones, _ = _gather_start([flat, summed["gates"]], None, "small_grad_gather_start")

    g, delta, new_m, new_v = {}, {}, {}, {}

    def update(n, g2, shape2):
        d2, m2, v2 = _adamw(w[n].reshape(shape2), g2, m[n].reshape(shape2), v[n].reshape(shape2),
                            "adamw_" + n)
        g[n], delta[n], new_m[n], new_v[n] = (a.reshape(w[n].shape) for a in (g2, d2, m2, v2))

    for n in ROW_SHARDED:
        update(n, summed[n], summed[n].shape)
    for n, t in TRANSPOSED.items():
        if t in summed:
            update(n, summed[t].T, summed[t].shape[::-1])

    sent, zones = _gather_wait(send_sems, recv_sems, sent, zones, range(len(sent)),
                               _after_all(list(delta.values()), "sharded_updates_done"),
                               "small_grad_gather_wait")
    small_parts, gate_sum = _gather_pass_on([a.shape[0] for a in sent], zones, SMALL_PASS_ON_ID,
                                            "small_grad_gather_pass_on")
    small = _sum_devices(small_parts, SMALL_ROWS, "small_grad_sum").reshape(-1)
    loss = small[n_small - 1]

    small_params = []
    o = 0
    for n, shape2 in REPLICATED:
        if n in GATE_BLOCKS:
            k, rows = GATE_BLOCKS.index(n), gate_sum.shape[0] // len(GATE_BLOCKS)
            update(n, gate_sum[k * rows:(k + 1) * rows].reshape(shape2), shape2)
        else:
            size = shape2[0] * shape2[1]
            small_params.append((n, small[o:o + size].reshape(shape2), shape2))
            o += size
    chan_back = summed["chan"].reshape(-1)[:N_CHAN_ROWS * LANES].reshape(N_CHAN_ROWS, LANES)
    r0 = 0
    for n, rows in CHAN:
        small_params.append((n, chan_back[r0:r0 + rows], (rows, LANES)))
        r0 += rows
    results = _adamw_small([(w[n].reshape(s2), g2, m[n].reshape(s2), v[n].reshape(s2))
                            for n, g2, s2 in small_params], "adamw_vectors")
    for (n, g2, _), (d2, m2, v2) in zip(small_params, results):
        g[n], delta[n], new_m[n], new_v[n] = (a.reshape(w[n].shape) for a in (g2, d2, m2, v2))

    summed = finish("late", late_sections, _after_all(list(delta.values()), "updates_done"), "late")
    g_t = summed["w_in_t"]
    results = _adamw(w["w_in"][0].T, g_t, m["w_in"][0].T, v["w_in"][0].T, "adamw_w_in")
    g["w_in"], delta["w_in"], new_m["w_in"], new_v["w_in"] = (a.T[None] for a in (g_t, *results))

    return (loss, grad_x[None], *[g[n] for n in NAMES], *[delta[n] for n in NAMES],
            *[new_m[n] for n in NAMES], *[new_v[n] for n in NAMES])
```

```python
import math

import numpy as np
import jax
import jax.numpy as jnp
from jax import lax
from jax.experimental import pallas as pl
from jax.experimental.pallas import tpu as pltpu

F32 = jnp.float32
BF16 = jnp.bfloat16

T = 2048
D = 1024
D_ATT = 512
D_REC = 1024
D_FF = 4096
D_IN = 5632
N_HEADS = 8
DH = 64
GRID_W = 64
ROWS = T // GRID_W
WIN_H = 8
WIN_W = 16
KWIN = WIN_H * GRID_W
N_RPB_R = 2 * WIN_H - 1
N_RPB_C = 2 * WIN_W - 1
N_REC_BLOCKS = 16
REC_BLOCK = 64
CG = 128
N_CG = D_REC // CG
LRU_C = 8.0
EPS = 1e-6
N_DEV = 8
N_CHIPS = 4
LANES = 128

ADAM_LR = 0.001
ADAM_B1 = 0.9
ADAM_B2 = 0.999
ADAM_EPS = 1e-08
ADAM_WD = 0.01
ADAM_STEP = 10

MESH_AXES = ("x", "y", "c")
VMEM_LIMIT = 56 * 1024 * 1024

TILE = 512
DZ_ARRAYS = ((0, 3, 1), (3, 4, 2), (7, 4, 2))
N_DZ_TILES = D_IN // TILE


def _params(**kw):
    return pltpu.CompilerParams(vmem_limit_bytes=VMEM_LIMIT, **kw)


HG = 4
HQ = HG * GRID_W
HC = HG * DH


def _att_tables():
    rq = np.arange(GRID_W)
    kc = np.arange(KWIN) % GRID_W
    win_start = np.clip(rq - WIN_W // 2, 0, GRID_W - WIN_W)
    valid = (kc[None, :] >= win_start[:, None]) & (kc[None, :] < win_start[:, None] + WIN_W)
    same_head = (np.arange(HQ)[:, None] // GRID_W) == (np.arange(HC)[None, :] // DH)
    return valid.astype(np.float32), same_head.astype(np.float32)


def _pair_mask():
    half = np.arange(2 * DH) // DH
    return (half[:, None] == half[None, :]).astype(np.float32)


def _dup_table():
    return np.concatenate([np.eye(REC_BLOCK, dtype=np.float32)] * 2, axis=1)


def _sigmoid(x):
    return 0.5 * jnp.tanh(0.5 * x) + 0.5


def _softplus(x):
    return jnp.maximum(x, 0.0) + jnp.log(1.0 + jnp.exp(-jnp.abs(x)))


def _one_minus_square(log_a, a):
    x = 2.0 * log_a
    series = -x * (1.0 + x * (0.5 + x * (1.0 / 6.0)))
    return jnp.where(x > -0.02, series, 1.0 - a * a)


_GELU_C = math.sqrt(2.0 / math.pi)


def _gelu_and_grad(x):
    x2 = x * x
    inner = _GELU_C * (x + 0.044715 * x * x2)
    t = jnp.tanh(inner)
    g = 0.5 * x * (1.0 + t)
    dg = 0.5 * (1.0 + t) + 0.5 * x * (1.0 - t * t) * _GELU_C * (1.0 + 3.0 * 0.044715 * x2)
    return g, dg


def _dot(a, b):
    return jnp.dot(a, b, preferred_element_type=F32)


def _dot_nt(a, b):
    return lax.dot_general(a, b, (((1,), (1,)), ((), ())), preferred_element_type=F32)


def _dot_tn(a, b):
    return lax.dot_general(a, b, (((0,), (0,)), ((), ())), preferred_element_type=F32)


def _dot_exact(a, b):
    return jnp.dot(a, b, precision=lax.Precision.HIGHEST, preferred_element_type=F32)


def _shift_rows(x, s):
    n = x.shape[0]
    rows = lax.broadcasted_iota(jnp.int32, x.shape, 0)
    y = pltpu.roll(x, s % n, 0)
    if s > 0:
        return jnp.where(rows >= s, y, 0.0)
    return jnp.where(rows < n + s, y, 0.0)


def _rms_bwd(dh, xh, r, g):
    dxh = dh * g
    return r * (dxh - xh * jnp.mean(dxh * xh, axis=-1, keepdims=True))


def _matmul(a, b, mode, out_dtype, name, tm=512, tn=1024, tk=2048):
    if mode == "nn":
        (m, k), (k2, n) = a.shape, b.shape
    elif mode == "nt":
        (m, k), (n, k2) = a.shape, b.shape
    else:
        (k, m), (k2, n) = a.shape, b.shape
    assert k == k2
    tm, tn, tk = min(tm, m), min(tn, n), min(tk, k)
    assert m % tm == 0 and n % tn == 0 and k % tk == 0
    nk = k // tk
    dot = {"nn": _dot, "nt": _dot_nt, "tn": _dot_tn}[mode]

    def body(a_ref, b_ref, o_ref, acc):
        kk = pl.program_id(2)
        part = dot(a_ref[...].astype(BF16), b_ref[...].astype(BF16))
        if nk == 1:
            o_ref[...] = part.astype(out_dtype)
            return

        @pl.when(kk == 0)
        def _():
            acc[...] = part

        @pl.when(kk > 0)
        def _():
            acc[...] += part

        @pl.when(kk == nk - 1)
        def _():
            o_ref[...] = acc[...].astype(out_dtype)

    if mode == "tn":
        a_spec = pl.BlockSpec((tk, tm), lambda i, j, kk: (kk, i))
    else:
        a_spec = pl.BlockSpec((tm, tk), lambda i, j, kk: (i, kk))
    if mode == "nt":
        b_spec = pl.BlockSpec((tn, tk), lambda i, j, kk: (j, kk))
    else:
        b_spec = pl.BlockSpec((tk, tn), lambda i, j, kk: (kk, j))
    return pl.pallas_call(
        body, name=name,
        out_shape=jax.ShapeDtypeStruct((m, n), out_dtype),
        grid=(m // tm, n // tn, nk),
        in_specs=[a_spec, b_spec],
        out_specs=pl.BlockSpec((tm, tn), lambda i, j, kk: (i, j)),
        scratch_shapes=[pltpu.VMEM((tm, tn) if nk > 1 else (8, LANES), F32)],
        compiler_params=_params(dimension_semantics=("parallel", "parallel", "arbitrary")),
    )(a, b)


def _in_proj(x, g1, w_in_t, b_in):
    tm = 512

    def body(x_ref, g_ref, w_hbm, b_ref, qkv_ref, uy_ref, gg_ref, h_ref, w):
        @pl.when(pl.program_id(0) == 0)
        def _():
            pltpu.sync_copy(w_hbm, w)

        xv = x_ref[...]
        r = lax.rsqrt(jnp.mean(xv * xv, axis=-1, keepdims=True) + EPS)
        h = ((xv * r) * g_ref[...]).astype(BF16)
        h_ref[...] = h
        row0 = 0
        for ref in (qkv_ref, uy_ref, gg_ref):
            for c0 in range(0, ref.shape[1], TILE):
                z = _dot_nt(h, w[row0:row0 + TILE, :]) + b_ref[:, row0:row0 + TILE]
                ref[:, c0:c0 + TILE] = z.astype(ref.dtype)
                row0 += TILE

    tok = lambda width: pl.BlockSpec((tm, width), lambda i: (i, 0))
    return pl.pallas_call(
        body, name="in_proj",
        out_shape=(jax.ShapeDtypeStruct((T, 3 * D_ATT), BF16),
                   jax.ShapeDtypeStruct((T, 2 * D_REC), F32),
                   jax.ShapeDtypeStruct((T, 2 * D), F32),
                   jax.ShapeDtypeStruct((T, D), BF16)),
        grid=(T // tm,),
        in_specs=[tok(D), pl.BlockSpec((1, D), lambda i: (0, 0)), pl.BlockSpec(memory_space=pl.ANY),
                  pl.BlockSpec((1, D_IN), lambda i: (0, 0))],
        out_specs=(tok(3 * D_ATT), tok(2 * D_REC), tok(2 * D), tok(D)),
        scratch_shapes=[pltpu.VMEM((D_IN, D), BF16)],
        compiler_params=_params(dimension_semantics=("arbitrary",)),
    )(x, g1, w_in_t, b_in)


def _dz_specs(rows, tile_of, row_of):
    def spec(off, n, per_plane):
        def index(*ids):
            t = jnp.clip(tile_of(*ids) - off, 0, n - 1)
            return (t // per_plane, row_of(*ids), t % per_plane)
        return pl.BlockSpec((1, rows, TILE), index)
    return [spec(off, n, per) for off, n, per in DZ_ARRAYS]


def _dh_norm1_bwd(dz, w_in_t, x, g1, dx1, after):
    tm = 512

    def body(dqkv_ref, duy_ref, dgg_ref, w_hbm, x_ref, g_ref, dx1_ref, after_ref, gx_ref, dg_ref, w):
        @pl.when(pl.program_id(0) == 0)
        def _():
            pltpu.sync_copy(w_hbm, w)
            dg_ref[...] = jnp.zeros_like(dg_ref)

        dh, row0 = None, 0
        for ref in (dqkv_ref, duy_ref, dgg_ref):
            for plane in range(ref.shape[0]):
                cols = ref.shape[2]
                part = _dot(ref[plane], w[row0:row0 + cols, :])
                dh = part if dh is None else dh + part
                row0 += cols
        xv = x_ref[...]
        r = lax.rsqrt(jnp.mean(xv * xv, axis=-1, keepdims=True) + EPS)
        xh = xv * r
        dg_ref[...] += jnp.sum(dh * xh, axis=0, keepdims=True)
        gx_ref[...] = dx1_ref[...] + _rms_bwd(dh, xh, r, g_ref[...])

    tok = pl.BlockSpec((tm, D), lambda i: (i, 0))
    vec = pl.BlockSpec((1, D), lambda i: (0, 0))
    planes = lambda a: pl.BlockSpec((a.shape[0], tm, a.shape[2]), lambda i: (0, i, 0))
    return pl.pallas_call(
        body, name="dh_norm1_bwd",
        out_shape=(jax.ShapeDtypeStruct((T, D), F32), jax.ShapeDtypeStruct((1, D), F32)),
        grid=(T // tm,),
        in_specs=[planes(a) for a in dz] + [pl.BlockSpec(memory_space=pl.ANY), tok, vec, tok,
                                            pl.BlockSpec(memory_space=pl.ANY)],
        out_specs=(tok, vec),
        scratch_shapes=[pltpu.VMEM((D_IN, D), BF16)],
        compiler_params=_params(dimension_semantics=("arbitrary",)),
    )(*dz, w_in_t, x, g1, dx1, after)


def _grad_w_in(dz, h):
    def body(*refs):
        seg_refs = refs[:3]
        h_ref, gw_ref, gb_ref = refs[3:]
        j = pl.program_id(0)

        for s, (off, n, _) in enumerate(DZ_ARRAYS):
            @pl.when((j >= off) & (j < off + n))
            def _(s=s):
                a = seg_refs[s][0]
                gw_ref[...] = _dot_tn(a, h_ref[...]).astype(BF16)
                gb_ref[...] = jnp.sum(a.astype(F32), axis=0, keepdims=True)

    return pl.pallas_call(
        body, name="grad_w_in",
        out_shape=(jax.ShapeDtypeStruct((D_IN, D), BF16), jax.ShapeDtypeStruct((1, D_IN), F32)),
        grid=(N_DZ_TILES,),
        in_specs=_dz_specs(T, lambda j: j, lambda j: 0) + [pl.BlockSpec((T, D), lambda j: (0, 0))],
        out_specs=(pl.BlockSpec((TILE, D), lambda j: (j, 0)), pl.BlockSpec((1, TILE), lambda j: (0, j))),
        compiler_params=_params(dimension_semantics=("parallel",)),
    )(*dz, h)


def _rpb_rows(rpb):
    padded = jnp.pad(rpb, ((0, 0), (0, 0), (0, GRID_W - N_RPB_C)))
    rows = [padded[:, WIN_H - 1 - oi: 2 * WIN_H - 1 - oi].reshape(N_HEADS // HG, HG, KWIN)
            for oi in range(WIN_H)]
    return jnp.stack(rows, axis=0)


SKEW = KWIN - (WIN_W - 1)


MASKED = -1e30


def _bias_tiles(rows_ref, valid, bias_s):
    for oi in range(WIN_H):
        for hh in range(HG):
            row = jnp.broadcast_to(rows_ref[oi, 0, hh:hh + 1, :], (GRID_W, KWIN))
            tile = pltpu.roll(row, SKEW, 1, stride=1, stride_axis=0)
            bias_s[oi, hh * GRID_W:(hh + 1) * GRID_W, :] = jnp.where(valid, tile, MASKED)


def _bias_tile_grads(gb_s, flip, out_ref):
    for oi in range(WIN_H):
        for hh in range(HG):
            g = _dot_exact(flip, gb_s[oi, hh * GRID_W:(hh + 1) * GRID_W, :])
            back = pltpu.roll(g, KWIN - (GRID_W - WIN_W), 1, stride=1, stride_axis=0)
            out_ref[0, oi, hh:hh + 1, :] = jnp.sum(back, axis=0, keepdims=True)


def _rpb_fold(row_grads):
    g = row_grads.transpose(1, 0, 2, 3).reshape(WIN_H, N_HEADS, WIN_H, GRID_W)
    g = g.transpose(0, 2, 1, 3)

    def body(g_ref, o_ref):
        for dr in range(N_RPB_R):
            terms = [g_ref[oi, i] for oi in range(WIN_H) for i in range(WIN_H) if i - oi + WIN_H - 1 == dr]
            acc = terms[0]
            for term in terms[1:]:
                acc = acc + term
            o_ref[dr] = acc

    out = pl.pallas_call(
        body, name="rpb_fold",
        out_shape=jax.ShapeDtypeStruct((N_RPB_R, N_HEADS, GRID_W), F32),
    )(g)
    return out.transpose(1, 0, 2)[:, :, :N_RPB_C]


ATT_GROUPS = N_HEADS // HG
ATT_UNROLL = 8


def _stacked(rows64, same_head):
    return jnp.where(same_head, jnp.concatenate([rows64] * HG, axis=0), jnp.zeros((), BF16))


def _own_heads(stacked):
    head = lax.broadcasted_iota(jnp.int32, (GRID_W, HC), 1) // DH
    out = stacked[:GRID_W]
    for h in range(1, HG):
        out = jnp.where(head == h, stacked[h * GRID_W:(h + 1) * GRID_W], out)
    return out


def _att_scores(q_ref, k_ref, bias_ref, same_head, r):
    rs = jnp.clip(r - WIN_H // 2, 0, ROWS - WIN_H)
    oi = r - rs
    q0 = pl.multiple_of(r * GRID_W, GRID_W)
    k0 = pl.multiple_of(rs * GRID_W, GRID_W)
    q2 = _stacked(q_ref[pl.ds(q0, GRID_W), :] * (DH ** -0.5), same_head)
    kw = k_ref[pl.ds(k0, KWIN), :]
    s = _dot_nt(q2, kw) + bias_ref[oi]
    e = jnp.exp(s - jnp.max(s, axis=-1, keepdims=True))
    return e, 1.0 / jnp.sum(e, axis=-1, keepdims=True), q2, kw, q0, k0, oi


def _att_specs():
    col = lambda off: pl.BlockSpec((T, HC), lambda g: (0, g + off * ATT_GROUPS))
    tables = [pl.BlockSpec((WIN_H, 1, HG, KWIN), lambda g: (0, g, 0, 0)),
              pl.BlockSpec((GRID_W, KWIN), lambda g: (0, 0)),
              pl.BlockSpec((HQ, HC), lambda g: (0, 0))]
    return col, tables, pltpu.VMEM((WIN_H, HQ, KWIN), F32)


def _att_fwd(qkv, bias_rows):
    valid_np, same_head_np = _att_tables()

    def body(q_ref, k_ref, v_ref, rows_ref, valid_ref, head_ref, o_ref, bias_s):
        same_head = head_ref[...] > 0.5
        _bias_tiles(rows_ref, valid_ref[...] > 0.5, bias_s)

        def row(r, carry):
            e, rl, _, _, q0, k0, _ = _att_scores(q_ref, k_ref, bias_s, same_head, r)
            o2 = _dot((e * rl).astype(BF16), v_ref[pl.ds(k0, KWIN), :])
            o_ref[pl.ds(q0, GRID_W), :] = _own_heads(o2).astype(BF16)
            return carry

        lax.fori_loop(0, ROWS, row, 0, unroll=ATT_UNROLL)

    col, tables, tiles = _att_specs()
    return pl.pallas_call(
        body, name="att_fwd",
        out_shape=jax.ShapeDtypeStruct((T, D_ATT), BF16),
        grid=(ATT_GROUPS,),
        in_specs=[col(0), col(1), col(2)] + tables,
        out_specs=col(0),
        scratch_shapes=[tiles],
        compiler_params=_params(dimension_semantics=("parallel",)),
    )(qkv, qkv, qkv, bias_rows, jnp.asarray(valid_np), jnp.asarray(same_head_np))


def _att_bwd(qkv, bias_rows, datt, after):
    valid_np, same_head_np = _att_tables()

    def body(q_ref, k_ref, v_ref, do_ref, rows_ref, valid_ref, head_ref, flip_ref,
             dqkv_ref, grows_ref, dk_acc, dv_acc, bias_s, gb_s):
        same_head = head_ref[...] > 0.5
        dk_acc[...] = jnp.zeros_like(dk_acc)
        dv_acc[...] = jnp.zeros_like(dv_acc)
        gb_s[...] = jnp.zeros_like(gb_s)
        _bias_tiles(rows_ref, valid_ref[...] > 0.5, bias_s)

        def row(r, carry):
            e, rl, q2, kw, q0, k0, oi = _att_scores(q_ref, k_ref, bias_s, same_head, r)
            do2 = _stacked(do_ref[pl.ds(q0, GRID_W), :], same_head)
            vw = v_ref[pl.ds(k0, KWIN), :]
            p = e * rl
            dp = _dot_nt(do2, vw)
            ds = p * (dp - jnp.sum(dp * p, axis=-1, keepdims=True))
            p16 = p.astype(BF16)
            ds16 = ds.astype(BF16)
            dv_acc[pl.ds(k0, KWIN), :] += _dot_tn(p16, do2)
            dk_acc[pl.ds(k0, KWIN), :] += _dot_tn(ds16, q2)
            dq2 = _dot(ds16, kw) * (DH ** -0.5)
            dqkv_ref[0, pl.ds(q0, GRID_W), :] = _own_heads(dq2).astype(BF16)
            gb_s[oi] += ds
            return carry

        lax.fori_loop(0, ROWS, row, 0, unroll=ATT_UNROLL)
        dqkv_ref[1] = dk_acc[...].astype(BF16)
        dqkv_ref[2] = dv_acc[...].astype(BF16)
        _bias_tile_grads(gb_s, flip_ref[...], grows_ref)

    col, tables, tiles = _att_specs()
    return pl.pallas_call(
        body, name="att_bwd",
        out_shape=(jax.ShapeDtypeStruct((3, T, D_ATT), BF16),
                   jax.ShapeDtypeStruct((ATT_GROUPS, WIN_H, HG, KWIN), F32)),
        grid=(ATT_GROUPS,),
        in_specs=[col(0), col(1), col(2), col(0)] + tables + [pl.BlockSpec((GRID_W, GRID_W), lambda g: (0, 0))],
        out_specs=(pl.BlockSpec((3, T, HC), lambda g: (0, 0, g)),
                   pl.BlockSpec((1, WIN_H, HG, KWIN), lambda g: (g, 0, 0, 0))),
        scratch_shapes=[pltpu.VMEM((T, HC), F32), pltpu.VMEM((T, HC), F32), tiles, tiles],
        compiler_params=_params(dimension_semantics=("parallel",)),
    )(qkv, qkv, qkv, datt, bias_rows, jnp.asarray(valid_np) + after, jnp.asarray(same_head_np),
      jnp.asarray(np.eye(GRID_W, dtype=np.float32)[::-1].copy()))


def _conv_taps(up):
    return (_shift_rows(up, 2), _shift_rows(up, 1), up, _shift_rows(up, -1))


def _pair_block_diag(w_pair, dup, same_half):
    return jnp.where(same_half, _dot(w_pair.astype(BF16), dup), 0.0).astype(BF16)


def _gates(u, u16, wa, ba, wi, bi, lam):
    r = _sigmoid(_dot(u16, wa) + ba)
    ig = _sigmoid(_dot(u16, wi) + bi)
    sp = _softplus(-lam)
    log_a = (-LRU_C) * r * sp
    a = jnp.exp(log_a)
    mult2 = jnp.maximum(_one_minus_square(log_a, a), 0.0)
    return r, ig, sp, a, jnp.sqrt(mult2), mult2


SCAN_BLOCKS = 8


def _scans(jobs):
    c = jobs[0][0].shape[1]
    nblk = T // 8
    rows = lax.broadcasted_iota(jnp.int32, (8, c), 0)

    def block(a, b, reverse):
        for s in (1, 2, 4):
            if reverse:
                keep = rows < 8 - s
                a_s = jnp.where(keep, pltpu.roll(a, 8 - s, 0), 1.0)
                b_s = jnp.where(keep, pltpu.roll(b, 8 - s, 0), 0.0)
            else:
                keep = rows >= s
                a_s = jnp.where(keep, pltpu.roll(a, s, 0), 1.0)
                b_s = jnp.where(keep, pltpu.roll(b, s, 0), 0.0)
            b = a * b_s + b
            a = a * a_s
        return a, b

    def step(i, carry):
        out = []
        for (a_ref, b_ref, h_ref, reverse), h_prev in zip(jobs, carry):
            for u in range(SCAN_BLOCKS):
                blk = i * SCAN_BLOCKS + u
                if reverse:
                    blk = nblk - 1 - blk
                t0 = pl.multiple_of(blk * 8, 8)
                a, b = block(a_ref[pl.ds(t0, 8), :], b_ref[pl.ds(t0, 8), :], reverse)
                h = a * h_prev + b
                h_ref[pl.ds(t0, 8), :] = h
                h_prev = jnp.broadcast_to(h[0:1] if reverse else h[7:8], (8, c))
            out.append(h_prev)
        return tuple(out)

    lax.fori_loop(0, nblk // SCAN_BLOCKS, step, tuple(jnp.zeros((8, c), F32) for _ in jobs))


def _rec_specs():
    tok = lambda off: pl.BlockSpec((T, CG), lambda g: (0, g + off))
    per_ch = lambda rows: pl.BlockSpec((rows, CG), lambda g: (0, g))
    wspec = pl.BlockSpec((2, 1, CG, REC_BLOCK), lambda g: (0, g, 0, 0))
    const = lambda shape: pl.BlockSpec(shape, lambda g: (0, 0))
    return tok, per_ch, wspec, const


def _rec_fwd(uy, conv_w, conv_b, w_a, b_a, w_i, b_i, lam):
    tok, per_ch, wspec, const = _rec_specs()

    def body(up_ref, yb_ref, cw_ref, cb_ref, wa_ref, ba_ref, wi_ref, bi_ref, lam_ref, dup_ref, half_ref,
             hf_ref, hb_ref, yrec_ref, am_ref, bx_f, bx_b):
        dup = dup_ref[...]
        same_half = half_ref[...] > 0.5
        taps = _conv_taps(up_ref[...])
        u = cb_ref[...]
        for j in range(4):
            u = u + taps[j] * cw_ref[j:j + 1, :]
        u16 = u.astype(BF16)
        for d, bx_s in enumerate((bx_f, bx_b)):
            wa = _pair_block_diag(wa_ref[d, 0], dup, same_half)
            wi = _pair_block_diag(wi_ref[d, 0], dup, same_half)
            _, ig, _, a, mult, _ = _gates(u, u16, wa, ba_ref[d:d + 1, :], wi, bi_ref[d:d + 1, :],
                                       lam_ref[d:d + 1, :])
            am_ref[2 * d] = a
            am_ref[2 * d + 1] = mult
            bx_s[...] = mult * (ig * u)
        _scans([(am_ref.at[0], bx_f, hf_ref, False), (am_ref.at[2], bx_b, hb_ref, True)])
        gelu, _ = _gelu_and_grad(yb_ref[...])
        yrec_ref[...] = ((hf_ref[...] + hb_ref[...]) * gelu).astype(BF16)

    return pl.pallas_call(
        body, name="rec_fwd",
        out_shape=(jax.ShapeDtypeStruct((T, D_REC), F32), jax.ShapeDtypeStruct((T, D_REC), F32),
                   jax.ShapeDtypeStruct((T, D_REC), BF16), jax.ShapeDtypeStruct((4, T, D_REC), F32)),
        grid=(N_CG,),
        in_specs=[tok(0), tok(N_CG), per_ch(4), per_ch(1), wspec, per_ch(2), wspec, per_ch(2), per_ch(2),
                  const((REC_BLOCK, CG)), const((CG, CG))],
        out_specs=(tok(0), tok(0), tok(0), pl.BlockSpec((4, T, CG), lambda g: (0, 0, g))),
        scratch_shapes=[pltpu.VMEM((T, CG), F32)] * 2,
        compiler_params=_params(dimension_semantics=("parallel",)),
    )(uy, uy, conv_w, conv_b, w_a, b_a, w_i, b_i, lam,
      jnp.asarray(_dup_table(), BF16), jnp.asarray(_pair_mask()))


def _rec_bwd(uy, hf, hb, am, dyrec, conv_w, conv_b, w_a, b_a, w_i, b_i, lam):
    tok, per_ch, wspec, const = _rec_specs()

    def body(up_ref, yb_ref, hf_ref, hb_ref, am_ref, dy_ref, cw_ref, cb_ref, wa_ref, ba_ref, wi_ref, bi_ref,
             lam_ref, dup_ref, dupt_ref, half_ref,
             duy_ref, dcw_ref, dcb_ref, dwa_ref, dba_ref, dwi_ref, dbi_ref, dlam_ref,
             a_s0, a_s1, dh_s, g_s0, g_s1):
        dup = dup_ref[...]
        dup_t = dupt_ref[...]
        same_half = half_ref[...] > 0.5
        taps = _conv_taps(up_ref[...])
        u = cb_ref[...]
        for j in range(4):
            u = u + taps[j] * cw_ref[j:j + 1, :]
        u16 = u.astype(BF16)
        gelu, dgelu = _gelu_and_grad(yb_ref[...])
        dy = dy_ref[...]
        duy_ref[1] = (dy * (hf_ref[...] + hb_ref[...]) * dgelu).astype(BF16)
        dh_s[...] = dy * gelu
        a_s0[...] = _shift_rows(am_ref[0], -1)
        a_s1[...] = _shift_rows(am_ref[2], 1)
        _scans([(a_s0, dh_s, g_s0, True), (a_s1, dh_s, g_s1, False)])
        du = jnp.zeros((T, CG), F32)
        for d, g_s in enumerate((g_s0, g_s1)):
            reverse = d == 1
            wa = _pair_block_diag(wa_ref[d, 0], dup, same_half)
            wi = _pair_block_diag(wi_ref[d, 0], dup, same_half)
            lam_d = lam_ref[d:d + 1, :]
            r = _sigmoid(_dot(u16, wa) + ba_ref[d:d + 1, :])
            ig = _sigmoid(_dot(u16, wi) + bi_ref[d:d + 1, :])
            sp = _softplus(-lam_d)
            a, mult = am_ref[2 * d], am_ref[2 * d + 1]
            mult2 = mult * mult
            g = g_s[...]
            h_prev = _shift_rows(hb_ref[...], -1) if reverse else _shift_rows(hf_ref[...], 1)
            da = g * h_prev
            dmult = g * (ig * u)
            dig = g * mult * u
            du = du + g * mult * ig
            dmult_dlog = jnp.where(mult2 > 0.0, -(a * a) * lax.rsqrt(mult2), 0.0)
            dlog_a = da * a + dmult * dmult_dlog
            dr = dlog_a * ((-LRU_C) * sp)
            dsp = jnp.sum(dlog_a * ((-LRU_C) * r), axis=0, keepdims=True)
            dlam_ref[d:d + 1, :] = dsp * (-_sigmoid(-lam_d))
            dga = dr * r * (1.0 - r)
            dgi = dig * ig * (1.0 - ig)
            dga16 = dga.astype(BF16)
            dgi16 = dgi.astype(BF16)
            du = du + _dot_nt(dga16, wa) + _dot_nt(dgi16, wi)
            dwa_ref[d, 0] = _dot_exact(jnp.where(same_half, _dot_tn(u16, dga16), 0.0), dup_t)
            dwi_ref[d, 0] = _dot_exact(jnp.where(same_half, _dot_tn(u16, dgi16), 0.0), dup_t)
            dba_ref[d:d + 1, :] = jnp.sum(dga, axis=0, keepdims=True)
            dbi_ref[d:d + 1, :] = jnp.sum(dgi, axis=0, keepdims=True)
        dcb_ref[...] = jnp.sum(du, axis=0, keepdims=True)
        for j in range(4):
            dcw_ref[j:j + 1, :] = jnp.sum(du * taps[j], axis=0, keepdims=True)
        dup_in = (_shift_rows(du, -2) * cw_ref[0:1, :] + _shift_rows(du, -1) * cw_ref[1:2, :]
                  + du * cw_ref[2:3, :] + _shift_rows(du, 1) * cw_ref[3:4, :])
        duy_ref[0] = dup_in.astype(BF16)

    wshape = jax.ShapeDtypeStruct((2, N_CG, CG, REC_BLOCK), F32)
    vec = lambda rows: jax.ShapeDtypeStruct((rows, D_REC), F32)
    dup_np = _dup_table()
    return pl.pallas_call(
        body, name="rec_bwd",
        out_shape=(jax.ShapeDtypeStruct((2, T, D_REC), BF16),
                   vec(4), vec(1), wshape, vec(2), wshape, vec(2), vec(2)),
        grid=(N_CG,),
        in_specs=[tok(0), tok(N_CG), tok(0), tok(0), pl.BlockSpec((4, T, CG), lambda g: (0, 0, g)), tok(0),
                  per_ch(4), per_ch(1), wspec, per_ch(2), wspec, per_ch(2), per_ch(2),
                  const((REC_BLOCK, CG)), const((CG, REC_BLOCK)), const((CG, CG))],
        out_specs=(pl.BlockSpec((2, T, CG), lambda g: (0, 0, g)),
                   per_ch(4), per_ch(1), wspec, per_ch(2), wspec, per_ch(2), per_ch(2)),
        scratch_shapes=[pltpu.VMEM((T, CG), F32)] * 5,
        compiler_params=_params(dimension_semantics=("parallel",)),
    )(uy, uy, hf, hb, am, dyrec, conv_w, conv_b, w_a, b_a, w_i, b_i, lam,
      jnp.asarray(dup_np, BF16), jnp.asarray(dup_np.T.copy()), jnp.asarray(_pair_mask()))


TM_MIX = 256


def _mix_specs():
    tok = lambda width, blk=0: pl.BlockSpec((TM_MIX, width), lambda i: (i, blk))
    full = lambda shape: pl.BlockSpec(shape, lambda i: (0, 0))
    return tok, full


def _mix_fwd(x, att, yrec, gg, w_att_o_t, w_rec_o, w_out):
    tok, full = _mix_specs()

    def body(x_ref, att_ref, yr_ref, ga_ref, gr_ref, wao_ref, wro_ref, wo_ref, x1_ref, mixed_ref):
        y_att = _dot_nt(att_ref[...], wao_ref[...])
        y_rec = _dot(yr_ref[...], wro_ref[...])
        mixed = (_sigmoid(ga_ref[...]) * y_att + _sigmoid(gr_ref[...]) * y_rec).astype(BF16)
        mixed_ref[...] = mixed
        x1_ref[...] = x_ref[...] + _dot(mixed, wo_ref[...])

    return pl.pallas_call(
        body, name="mix_fwd",
        out_shape=(jax.ShapeDtypeStruct((T, D), F32), jax.ShapeDtypeStruct((T, D), BF16)),
        grid=(T // TM_MIX,),
        in_specs=[tok(D), tok(D_ATT), tok(D_REC), tok(D, 0), tok(D, 1),
                  full((D, D_ATT)), full((D_REC, D)), full((D, D))],
        out_specs=(tok(D), tok(D)),
        compiler_params=_params(dimension_semantics=("parallel",)),
    )(x, att, yrec, gg, gg, w_att_o_t, w_rec_o, w_out)


def _mix_bwd(dx1, att, yrec, gg, w_att_o_t, w_rec_o, w_out, after):
    tok, full = _mix_specs()

    def body(dx_ref, att_ref, yr_ref, ga_ref, gr_ref, wao_ref, wro_ref, wo_ref, after_ref,
             dgg_ref, dya_ref, dyr_ref, datt_ref, dyrp_ref):
        dmixed = _dot_nt(dx_ref[...].astype(BF16), wo_ref[...])
        y_att = _dot_nt(att_ref[...], wao_ref[...])
        y_rec = _dot(yr_ref[...], wro_ref[...])
        sa = _sigmoid(ga_ref[...])
        sr = _sigmoid(gr_ref[...])
        dgg_ref[0] = (dmixed * y_att * sa * (1.0 - sa)).astype(BF16)
        dgg_ref[1] = (dmixed * y_rec * sr * (1.0 - sr)).astype(BF16)
        dya = (dmixed * sa).astype(BF16)
        dyr = (dmixed * sr).astype(BF16)
        dya_ref[...] = dya
        dyr_ref[...] = dyr
        datt_ref[...] = _dot(dya, wao_ref[...]).astype(BF16)
        dyrp_ref[...] = _dot_nt(dyr, wro_ref[...])

    return pl.pallas_call(
        body, name="mix_bwd",
        out_shape=(jax.ShapeDtypeStruct((2, T, D), BF16),
                   jax.ShapeDtypeStruct((T, D), BF16), jax.ShapeDtypeStruct((T, D), BF16),
                   jax.ShapeDtypeStruct((T, D_ATT), BF16), jax.ShapeDtypeStruct((T, D_REC), F32)),
        grid=(T // TM_MIX,),
        in_specs=[tok(D), tok(D_ATT), tok(D_REC), tok(D, 0), tok(D, 1),
                  full((D, D_ATT)), full((D_REC, D)), full((D, D)), pl.BlockSpec(memory_space=pl.ANY)],
        out_specs=(pl.BlockSpec((2, TM_MIX, D), lambda i: (0, i, 0)),
                   tok(D), tok(D), tok(D_ATT), tok(D_REC)),
        compiler_params=_params(dimension_semantics=("parallel",)),
    )(dx1, att, yrec, gg, gg, w_att_o_t, w_rec_o, w_out, after)


TM_FFN = 256
FF_CHUNK = 1024


def _ffn_loss(x1, target, g2, gf, w_ff1_t, w_ff2):
    n_chunks = D_FF // FF_CHUNK

    def body(x1_ref, tg_ref, g2_ref, gf_ref, w1_hbm, w2_hbm,
             loss_ref, dx1_ref, h2_ref, act_ref, dpre_ref, dx2_ref, dg2_ref, dgf_ref,
             w1, w2, relu_s):
        i = pl.program_id(0)

        @pl.when(i == 0)
        def _():
            pltpu.sync_copy(w1_hbm, w1)
            pltpu.sync_copy(w2_hbm, w2)
            loss_ref[...] = jnp.zeros_like(loss_ref)
            dg2_ref[...] = jnp.zeros_like(dg2_ref)
            dgf_ref[...] = jnp.zeros_like(dgf_ref)

        x1v = x1_ref[...]
        r2 = lax.rsqrt(jnp.mean(x1v * x1v, axis=-1, keepdims=True) + EPS)
        xh2 = x1v * r2
        h2 = (xh2 * g2_ref[...]).astype(BF16)
        h2_ref[...] = h2
        x2 = x1v
        for c in range(n_chunks):
            ff = slice(c * FF_CHUNK, (c + 1) * FF_CHUNK)
            rl = jnp.maximum(_dot_nt(h2, w1[ff, :]), 0.0)
            relu_s[:, ff] = rl
            act = (rl * rl).astype(BF16)
            act_ref[:, ff] = act
            x2 = x2 + _dot(act, w2[ff, :])
        r3 = lax.rsqrt(jnp.mean(x2 * x2, axis=-1, keepdims=True) + EPS)
        xh3 = x2 * r3
        err = xh3 * gf_ref[...] - tg_ref[...]
        loss_ref[...] += 0.5 * jnp.sum(jnp.mean(err * err, axis=-1, keepdims=True))
        dy = err * (1.0 / D)
        dgf_ref[...] += jnp.sum(dy * xh3, axis=0, keepdims=True)
        dx2 = _rms_bwd(dy, xh3, r3, gf_ref[...])
        dx2_16 = dx2.astype(BF16)
        dx2_ref[...] = dx2_16
        dh2 = jnp.zeros((TM_FFN, D), F32)
        for c in range(n_chunks):
            ff = slice(c * FF_CHUNK, (c + 1) * FF_CHUNK)
            dpre = (_dot_nt(dx2_16, w2[ff, :]) * (2.0 * relu_s[:, ff])).astype(BF16)
            dpre_ref[:, ff] = dpre
            dh2 = dh2 + _dot(dpre, w1[ff, :])
        dg2_ref[...] += jnp.sum(dh2 * xh2, axis=0, keepdims=True)
        dx1_ref[...] = dx2 + _rms_bwd(dh2, xh2, r2, g2_ref[...])

    tok = lambda width: pl.BlockSpec((TM_FFN, width), lambda i: (i, 0))
    vec = pl.BlockSpec((1, D), lambda i: (0, 0))
    hbm = pl.BlockSpec(memory_space=pl.ANY)
    return pl.pallas_call(
        body, name="ffn_loss",
        out_shape=(jax.ShapeDtypeStruct((8, 128), F32), jax.ShapeDtypeStruct((T, D), F32),
                   jax.ShapeDtypeStruct((T, D), BF16), jax.ShapeDtypeStruct((T, D_FF), BF16),
                   jax.ShapeDtypeStruct((T, D_FF), BF16), jax.ShapeDtypeStruct((T, D), BF16),
                   jax.ShapeDtypeStruct((1, D), F32), jax.ShapeDtypeStruct((1, D), F32)),
        grid=(T // TM_FFN,),
        in_specs=[tok(D), tok(D), vec, vec, hbm, hbm],
        out_specs=(pl.BlockSpec((8, 128), lambda i: (0, 0)), tok(D), tok(D), tok(D_FF), tok(D_FF), tok(D),
                   vec, vec),
        scratch_shapes=[pltpu.VMEM((D_FF, D), BF16), pltpu.VMEM((D_FF, D), BF16),
                        pltpu.VMEM((TM_FFN, D_FF), F32)],
        compiler_params=_params(dimension_semantics=("arbitrary",)),
    )(x1, target, g2, gf, w_ff1_t, w_ff2)


def _local_step(x, target, p, late_weights, reduce_first, reduce_early, reduce_late):
    bias = _rpb_rows(p["rpb"])
    pairs = lambda w: w.reshape(2, N_CG, CG, REC_BLOCK)
    w_a, w_i = pairs(p["w_rg_a"]), pairs(p["w_rg_i"])
    rec_params = (p["conv_w"], p["conv_b"], w_a, p["b_rg_a"], w_i, p["b_rg_i"], p["lru_lambda"])

    qkv, uy, gg, h = _in_proj(x, p["ln1_g"], p["w_in_t"], p["b_in"])
    att = _att_fwd(qkv, bias)
    hf, hb, yrec, am = _rec_fwd(uy, *rec_params)
    p = {**p, **late_weights(yrec, 0)}
    x1, mixed = _mix_fwd(x, att, yrec, gg, p["w_att_o_t"], p["w_rec_o"], p["w_out"])
    p = {**p, **late_weights(x1, 1)}
    loss8, dx1, h2, act, dpre, dx2, g_ln2, g_lnf = _ffn_loss(
        x1, target, p["ln2_g"], p["lnf_g"], p["w_ff1_t"], p["w_ff2"])

    grads = {"ln2_g": g_ln2, "lnf_g": g_lnf,
             "w_ff1_t": _matmul(dpre, h2, "tn", BF16, "g_w_ff1"),
             "w_ff2": _matmul(act, dx2, "tn", BF16, "g_w_ff2")}
    dgg, dya, dyr, datt, dyrp = _mix_bwd(dx1, att, yrec, gg, p["w_att_o_t"], p["w_rec_o"], p["w_out"],
                                         reduce_first(grads, None))
    lam_after = rec_params[-1] + reduce_first(None, dgg)[0, 0]
    duy, g_cw, g_cb, g_wa, g_ba, g_wi, g_bi, g_lam = _rec_bwd(uy, hf, hb, am, dyrp, *rec_params[:-1], lam_after)
    blocks = lambda g: g.reshape(2, N_REC_BLOCKS, REC_BLOCK, REC_BLOCK)
    grads.update({
        "w_att_o_t": _matmul(dya, att, "tn", BF16, "g_w_att_o"),
        "conv_w": g_cw, "conv_b": g_cb, "w_rg_a": blocks(g_wa), "b_rg_a": g_ba,
        "w_rg_i": blocks(g_wi), "b_rg_i": g_bi, "lru_lambda": g_lam,
        "w_rec_o": _matmul(yrec, dyr, "tn", BF16, "g_w_rec_o"),
        "w_out": _matmul(mixed, dx1, "tn", BF16, "g_w_out"),
    })
    dqkv, gbias = _att_bwd(qkv, bias, datt, reduce_early(grads))
    dz = (dqkv, duy, dgg)
    g_w_in_t, g_b_in = _grad_w_in(dz, h)
    grads.update(w_in_t=g_w_in_t, b_in=g_b_in)
    grad_x, g_ln1 = _dh_norm1_bwd(dz, p["w_in_t"], x, p["ln1_g"], dx1, reduce_late(grads))
    grads.update(ln1_g=g_ln1, rpb=_rpb_fold(gbias))
    return loss8[0:1, 0:1], grad_x, grads


MESH_ID = pl.DeviceIdType.MESH
ANY = pl.BlockSpec(memory_space=pl.ANY)

CHAN_BLOCK_ROWS = 32
GATE_ROWS = 2 * 2 * N_REC_BLOCKS * REC_BLOCK * REC_BLOCK // (N_DEV * D)
SECTIONS = (("w_in_t", 704, D), ("w_rec_o", 128, D), ("w_out", 128, D), ("w_ff1_t", 512, D),
            ("w_ff2", 512, D), ("chan", CHAN_BLOCK_ROWS, D), ("w_att_o_t", 128, D_ATT),
            ("gates", GATE_ROWS, D))
N_SEC = len(SECTIONS)
N_CHAN_ROWS = 10
CHAN = (("conv_w", 4), ("b_rg_a", 2), ("b_rg_i", 2), ("lru_lambda", 2))


def _position():
    return lax.axis_index("x"), lax.axis_index("y"), lax.axis_index("c")


def _other_chips(x, y):
    return [(1 - x, y), (x, 1 - y), (1 - x, 1 - y)]


PASS_ON_IDS, PAIR_EARLY_ID, PAIR_LATE_ID, PAIR_FIRST_ID, SMALL_PASS_ON_ID = (1, 4), 2, 3, 5, 6


def _pair_handshake(x, y, c):
    barrier = pltpu.get_barrier_semaphore()
    pl.semaphore_signal(barrier, inc=1, device_id=(x, y, 1 - c), device_id_type=MESH_ID)
    pl.semaphore_wait(barrier, 1)


def _block_of(ref, dev, rows):
    return ref.at[pl.ds(pl.multiple_of(dev * rows, 16), rows)]


def _all_gather(shards, name):
    ns = len(shards)

    def body(*refs):
        x_refs, out_refs, done_ref = refs[:ns], refs[ns:2 * ns], refs[2 * ns]
        send_sems, recv_sems, local_sems = refs[2 * ns + 1:]
        done_ref[0, 0] = 0.0
        x, y, c = _position()
        me, sibling = (x, y, c), (x, y, 1 - c)
        x_nbr, y_nbr, diagonal = _other_chips(x, y)
        north = c == 1
        relay_from = (jnp.where(north, x_nbr[0], y_nbr[0]), jnp.where(north, x_nbr[1], y_nbr[1]))
        relay_to = (jnp.where(north, y_nbr[0], x_nbr[0]), jnp.where(north, y_nbr[1], x_nbr[1]))

        def rows(s, px, py, pc):
            return _block_of(out_refs[s], 4 * px + 2 * py + pc, shards[s].shape[0])

        def copy(k, s, block, to, from_shard=False):
            return pltpu.make_async_remote_copy(
                src_ref=x_refs[s] if from_shard else rows(s, *block), dst_ref=rows(s, *block),
                send_sem=send_sems.at[k * ns + s], recv_sem=recv_sems.at[k * ns + s],
                device_id=to, device_id_type=MESH_ID)

        sections = range(ns)
        mine = [pltpu.make_async_copy(x_refs[s], rows(s, *me), local_sems.at[s]) for s in sections]
        sent = [copy(k, s, me, to, True) for k, to in enumerate((sibling, (*x_nbr, c), (*y_nbr, c)))
                for s in sections]
        for cp in mine + sent:
            cp.start()
        for s in sections:
            copy(1, s, (*x_nbr, c), me).wait_recv()
            copy(2, s, (*y_nbr, c), me).wait_recv()
            sent += [copy(3, s, (*relay_from, c), (*relay_to, c)),
                     copy(4, s, (*x_nbr, c), sibling), copy(5, s, (*y_nbr, c), sibling)]
            for cp in sent[-3:]:
                cp.start()
        for s in sections:
            copy(3, s, (*diagonal, c), me).wait_recv()
            sent.append(copy(6, s, (*diagonal, c), sibling))
            sent[-1].start()
        for s in sections:
            copy(0, s, sibling, me).wait_recv()
            for k, chip in ((4, x_nbr), (5, y_nbr), (6, diagonal)):
                copy(k, s, (*chip, 1 - c), me).wait_recv()
        for cp in sent:
            cp.wait_send()
        for cp in mine:
            cp.wait()

    return pl.pallas_call(
        body, name=name,
        out_shape=tuple(jax.ShapeDtypeStruct((N_DEV * s.shape[0], s.shape[1]), s.dtype) for s in shards)
        + (jax.ShapeDtypeStruct((1, 1), F32),),
        in_specs=[ANY] * ns,
        out_specs=(ANY,) * ns + (pl.BlockSpec(memory_space=pltpu.SMEM),),
        scratch_shapes=[pltpu.SemaphoreType.DMA((7 * ns,)), pltpu.SemaphoreType.DMA((7 * ns,)),
                        pltpu.SemaphoreType.DMA((ns,))],
    )(*shards)


HBM = pl.BlockSpec(memory_space=pltpu.HBM)
SEM = pl.BlockSpec(memory_space=pltpu.SEMAPHORE)
EFFECT = pltpu.SideEffectType.DATAFLOW_SIDE_EFFECTING


def _in_hbm(a):
    return pltpu.with_memory_space_constraint(a, pltpu.HBM)


def _first_hop_copies(shards, x_refs, zones, send_sems, recv_sems):
    ns = len(shards)
    x, y, c = _position()
    targets = [(x, y, 1 - c)] + [(cx, cy, c) for cx, cy in _other_chips(x, y)]
    return [pltpu.make_async_remote_copy(
        src_ref=x_refs[s], dst_ref=_block_of(zones[s], 4 * x + 2 * y + c, shards[s].shape[0]),
        send_sem=send_sems.at[k * ns + s], recv_sem=recv_sems.at[k * ns + s],
        device_id=to, device_id_type=MESH_ID)
        for k, to in enumerate(targets) for s in range(ns)]


def _after_all(arrays, name):
    def body(*refs):
        refs[-1][...] = jnp.zeros_like(refs[-1])

    return pl.pallas_call(
        body, name=name,
        out_shape=jax.ShapeDtypeStruct((8, LANES), F32),
        in_specs=[pl.BlockSpec(memory_space=pl.ANY)] * len(arrays),
        out_specs=pl.BlockSpec(memory_space=pltpu.VMEM),
    )(*arrays)


def _own_blocks_placed(shards, after, name):
    ns = len(shards)
    x, y, c = _position()
    me = jnp.reshape(4 * x + 2 * y + c, (1,)).astype(jnp.int32)
    if after is not None:
        shards = [*shards[:-1], shards[-1] + after.astype(shards[-1].dtype)]

    def body(me_ref, *refs):
        for s in range(ns):
            refs[ns + s][...] = refs[s][...]

    return pl.pallas_call(
        body, name=name,
        out_shape=tuple(jax.ShapeDtypeStruct((N_DEV * s.shape[0], s.shape[1]), s.dtype) for s in shards),
        grid_spec=pltpu.PrefetchScalarGridSpec(
            num_scalar_prefetch=1, grid=(1,),
            in_specs=[pl.BlockSpec(s.shape, lambda i, me: (0, 0)) for s in shards],
            out_specs=tuple(pl.BlockSpec(s.shape, lambda i, me: (me[0], 0)) for s in shards)),
        compiler_params=_params(dimension_semantics=("arbitrary",)),
    )(me, *shards)


def _gather_start(shards, after, name):
    ns = len(shards)
    zones = _own_blocks_placed(shards, after, name + "_own_blocks")

    def body(*refs):
        for cp in _first_hop_copies(shards, refs[:ns], refs[ns:2 * ns], refs[2 * ns], refs[2 * ns + 1]):
            cp.start()
        refs[-1][...] = jnp.zeros_like(refs[-1])

    out = pl.pallas_call(
        body, name=name,
        out_shape=(pltpu.SemaphoreType.DMA((4 * ns,)), pltpu.SemaphoreType.DMA((4 * ns,)),
                   *[pltpu.HBM(a.shape, a.dtype) for a in (*shards, *zones)],
                   jax.ShapeDtypeStruct((8, LANES), F32)),
        in_specs=[HBM] * (2 * ns),
        out_specs=(SEM, SEM, *[HBM] * (2 * ns), pl.BlockSpec(memory_space=pltpu.VMEM)),
        input_output_aliases={i: 2 + i for i in range(2 * ns)},
        compiler_params=pltpu.CompilerParams(has_side_effects=EFFECT),
    )(*[_in_hbm(a) for a in shards], *[_in_hbm(a) for a in zones])
    return out[0], out[1], out[2:2 + ns], out[2 + ns:2 + 2 * ns], out[-1]


def _gather_wait(send_sems, recv_sems, shards, zones, which, after, name):
    ns = len(shards)

    def body(*refs):
        copies = _first_hop_copies(shards, refs[:ns], refs[ns:2 * ns], refs[2 * ns], refs[2 * ns + 1])
        for i, cp in enumerate(copies):
            if i % ns in which:
                cp.wait_send()
                cp.wait_recv()

    out = pl.pallas_call(
        body, name=name,
        out_shape=tuple(pltpu.HBM(a.shape, a.dtype) for a in (*shards, *zones)),
        in_specs=[HBM] * (2 * ns) + [SEM, SEM, ANY],
        out_specs=(HBM,) * (2 * ns),
        input_output_aliases={i: i for i in range(2 * ns)},
        compiler_params=pltpu.CompilerParams(has_side_effects=EFFECT),
    )(*shards, *zones, send_sems, recv_sems, after)
    return out[:ns], out[ns:]


def _gather_pass_on(rows, zones, barrier_id, name):
    ns = len(zones)

    def body(*refs):
        in_refs, out_refs = refs[:ns], refs[ns:2 * ns]
        send_sems, recv_sems = refs[2 * ns:]
        x, y, c = _position()
        _pair_handshake(x, y, c)
        copies = [pltpu.make_async_remote_copy(
            src_ref=_block_of(in_refs[s], 4 * cx + 2 * cy + c, rows[s]),
            dst_ref=_block_of(out_refs[s], 4 * cx + 2 * cy + c, rows[s]),
            send_sem=send_sems.at[j * ns + s], recv_sem=recv_sems.at[j * ns + s],
            device_id=(x, y, 1 - c), device_id_type=MESH_ID)
            for j, (cx, cy) in enumerate(_other_chips(x, y)) for s in range(ns)]
        for cp in copies:
            cp.start()
        for cp in copies:
            cp.wait_recv()
        for cp in copies:
            cp.wait_send()

    return pl.pallas_call(
        body, name=name,
        out_shape=tuple(jax.ShapeDtypeStruct(z.shape, z.dtype) for z in zones),
        in_specs=[ANY] * ns, out_specs=(ANY,) * ns,
        input_output_aliases={i: i for i in range(ns)},
        scratch_shapes=[pltpu.SemaphoreType.DMA((3 * ns,)), pltpu.SemaphoreType.DMA((3 * ns,))],
        compiler_params=pltpu.CompilerParams(collective_id=barrier_id),
    )(*zones)


def _pair_copies(sections, g_refs, land, send_sems, recv_sems):
    ns = len(sections)
    x, y, c = _position()
    return [pltpu.make_async_remote_copy(
        src_ref=_block_of(g_refs[s], 2 * k + 1 - c, rows), dst_ref=land[s].at[k],
        send_sem=send_sems.at[k * ns + s], recv_sem=recv_sems.at[k * ns + s],
        device_id=(x, y, 1 - c), device_id_type=MESH_ID)
        for k in range(N_CHIPS) for s, (_, rows, _) in enumerate(sections)]


def _pair_exchange_start(sections, grads, barrier_id, name):
    ns = len(sections)

    def body(*refs):
        _pair_handshake(*_position())
        for cp in _pair_copies(sections, refs[:ns], refs[ns:2 * ns], refs[2 * ns], refs[2 * ns + 1]):
            cp.start()
        refs[-1][...] = jnp.zeros_like(refs[-1])

    zones = [lax.empty((N_CHIPS, rows, cols), BF16) for _, rows, cols in sections]
    n = N_CHIPS * ns
    out = pl.pallas_call(
        body, name=name,
        out_shape=(pltpu.SemaphoreType.DMA((n,)), pltpu.SemaphoreType.DMA((n,)),
                   *[pltpu.HBM(a.shape, a.dtype) for a in (*grads, *zones)],
                   jax.ShapeDtypeStruct((8, LANES), F32)),
        in_specs=[HBM] * (2 * ns),
        out_specs=(SEM, SEM, *[HBM] * (2 * ns), pl.BlockSpec(memory_space=pltpu.VMEM)),
        input_output_aliases={i: 2 + i for i in range(2 * ns)},
        compiler_params=pltpu.CompilerParams(has_side_effects=EFFECT, collective_id=barrier_id),
    )(*[_in_hbm(a) for a in grads], *[_in_hbm(a) for a in zones])
    return out[0], out[1], out[2:2 + ns], out[2 + ns:2 + 2 * ns], out[-1]


def _pair_exchange_wait(sections, send_sems, recv_sems, grads, zones, after, name):
    ns = len(sections)

    def body(*refs):
        for cp in _pair_copies(sections, refs[:ns], refs[ns:2 * ns], refs[2 * ns], refs[2 * ns + 1]):
            cp.wait_send()
            cp.wait_recv()

    out = pl.pallas_call(
        body, name=name,
        out_shape=tuple(pltpu.HBM(a.shape, a.dtype) for a in (*grads, *zones)),
        in_specs=[HBM] * (2 * ns) + [SEM, SEM, ANY],
        out_specs=(HBM,) * (2 * ns),
        input_output_aliases={i: i for i in range(2 * ns)},
        compiler_params=pltpu.CompilerParams(has_side_effects=EFFECT),
    )(*grads, *zones, send_sems, recv_sems, after)
    return out[:ns], out[ns:]


def _pair_add(sections, grads, got, core, name):
    ns = len(sections)

    def body(core_ref, *refs):
        g_refs, got_refs, p_refs = refs[:ns], refs[ns:2 * ns], refs[2 * ns:]
        for s in range(ns):
            p_refs[s][0] = (g_refs[s][...].astype(F32) + got_refs[s][0].astype(F32)).astype(BF16)

    slot = [pl.BlockSpec((1, rows, cols), lambda k, c: (k, 0, 0)) for _, rows, cols in sections]
    return pl.pallas_call(
        body, name=name,
        out_shape=tuple(jax.ShapeDtypeStruct((N_CHIPS, rows, cols), BF16) for _, rows, cols in sections),
        grid_spec=pltpu.PrefetchScalarGridSpec(
            num_scalar_prefetch=1, grid=(N_CHIPS,),
            in_specs=[pl.BlockSpec((rows, cols), lambda k, c: (2 * k + c[0], 0)) for _, rows, cols in sections]
            + slot,
            out_specs=tuple(slot)),
        compiler_params=_params(dimension_semantics=("parallel",)),
    )(core, *grads, *got)


def _chip_copies(sections, p_refs, land, send_sems, recv_sems):
    ns = len(sections)
    x, y, c = _position()
    return [pltpu.make_async_remote_copy(
        src_ref=p_refs[s].at[2 * cx + cy], dst_ref=land[s].at[j],
        send_sem=send_sems.at[j * ns + s], recv_sem=recv_sems.at[j * ns + s],
        device_id=(cx, cy, c), device_id_type=MESH_ID)
        for j, (cx, cy) in enumerate(_other_chips(x, y)) for s in range(ns)]


def _chip_exchange(sections, parts, name):
    ns = len(sections)

    def body(*refs):
        copies = _chip_copies(sections, refs[:ns], refs[ns:2 * ns], *refs[2 * ns:])
        for cp in copies:
            cp.start()
        for cp in copies:
            cp.wait_recv()
        for cp in copies:
            cp.wait_send()

    n = 3 * ns
    return pl.pallas_call(
        body, name=name,
        out_shape=tuple(jax.ShapeDtypeStruct((3, rows, cols), BF16) for _, rows, cols in sections),
        in_specs=[ANY] * ns, out_specs=(ANY,) * ns,
        scratch_shapes=[pltpu.SemaphoreType.DMA((n,)), pltpu.SemaphoreType.DMA((n,))],
    )(*parts)


def _chip_exchange_start(sections, parts, name):
    ns = len(sections)

    def body(*refs):
        p_refs, land = refs[:ns], refs[ns:2 * ns]
        send_sems, recv_sems = refs[2 * ns], refs[2 * ns + 1]
        token = refs[-1]
        for cp in _chip_copies(sections, p_refs, land, send_sems, recv_sems):
            cp.start()
        token[...] = jnp.zeros_like(token)

    zones = [lax.empty((3, rows, cols), BF16) for _, rows, cols in sections]
    out = pl.pallas_call(
        body, name=name,
        out_shape=(pltpu.SemaphoreType.DMA((3 * ns,)), pltpu.SemaphoreType.DMA((3 * ns,)),
                   *[pltpu.HBM(a.shape, a.dtype) for a in parts], *[pltpu.HBM(a.shape, a.dtype) for a in zones],
                   jax.ShapeDtypeStruct((8, LANES), F32)),
        in_specs=[HBM] * (2 * ns),
        out_specs=(SEM, SEM, *[HBM] * (2 * ns), pl.BlockSpec(memory_space=pltpu.VMEM)),
        input_output_aliases={i: 2 + i for i in range(2 * ns)},
        compiler_params=pltpu.CompilerParams(has_side_effects=EFFECT),
    )(*[_in_hbm(a) for a in parts], *[_in_hbm(a) for a in zones])
    return out[0], out[1], out[2:2 + ns], out[2 + ns:2 + 2 * ns], out[-1]


def _chip_exchange_wait(sections, send_sems, recv_sems, parts, zones, after, name):
    ns = len(sections)

    def body(*refs):
        p_refs, land = refs[:ns], refs[ns:2 * ns]
        for cp in _chip_copies(sections, p_refs, land, refs[2 * ns], refs[2 * ns + 1]):
            cp.wait_send()
            cp.wait_recv()

    out = pl.pallas_call(
        body, name=name,
        out_shape=tuple(pltpu.HBM(a.shape, a.dtype) for a in (*parts, *zones)),
        in_specs=[HBM] * (2 * ns) + [SEM, SEM, ANY],
        out_specs=(HBM,) * (2 * ns),
        input_output_aliases={i: i for i in range(2 * ns)},
        compiler_params=pltpu.CompilerParams(has_side_effects=EFFECT),
    )(*parts, *zones, send_sems, recv_sems, after)
    return out[:ns], out[ns:]


def _grad_finish(sections, parts, far, chip, name):
    ns = len(sections)

    def body(chip_ref, *refs):
        p_refs, b_refs, g_refs = refs[:ns], refs[ns:2 * ns], refs[2 * ns:]
        for s in range(ns):
            g = p_refs[s][0].astype(F32)
            for j in range(3):
                g = g + b_refs[s][j].astype(F32)
            g_refs[s][...] = g

    half = [(rows // 2, cols) for _, rows, cols in sections]
    return pl.pallas_call(
        body, name=name,
        out_shape=tuple(jax.ShapeDtypeStruct((rows, cols), F32) for _, rows, cols in sections),
        grid_spec=pltpu.PrefetchScalarGridSpec(
            num_scalar_prefetch=1, grid=(2,),
            in_specs=[pl.BlockSpec((1, r, c), lambda i, chip: (chip[0], i, 0)) for r, c in half]
            + [pl.BlockSpec((3, r, c), lambda i, chip: (0, i, 0)) for r, c in half],
            out_specs=tuple(pl.BlockSpec((r, c), lambda i, chip: (i, 0)) for r, c in half)),
        compiler_params=_params(dimension_semantics=("parallel",)),
    )(chip, *parts, *far)


def _sum_devices(parts, rows, name):
    cols = parts.shape[1]
    tr = rows // 2

    def body(*refs):
        s = refs[0][...].astype(F32)
        for d in range(1, N_DEV):
            s = s + refs[d][...].astype(F32)
        refs[N_DEV][...] = s

    return pl.pallas_call(
        body, name=name,
        out_shape=jax.ShapeDtypeStruct((rows, cols), F32),
        grid=(2,),
        in_specs=[pl.BlockSpec((tr, cols), lambda i, d=d: (2 * d + i, 0)) for d in range(N_DEV)],
        out_specs=pl.BlockSpec((tr, cols), lambda i: (i, 0)),
        compiler_params=_params(dimension_semantics=("parallel",)),
    )(*([parts] * N_DEV))


def _adamw_step(w_ref, g_ref, m_ref, v_ref, d_ref, nm_ref, nv_ref):
    c1 = 1.0 / (1.0 - ADAM_B1 ** ADAM_STEP)
    c2 = 1.0 / (1.0 - ADAM_B2 ** ADAM_STEP)
    gv = g_ref[...]
    nm = ADAM_B1 * m_ref[...] + (1.0 - ADAM_B1) * gv
    nv = ADAM_B2 * v_ref[...] + (1.0 - ADAM_B2) * (gv * gv)
    nm_ref[...] = nm
    nv_ref[...] = nv
    d_ref[...] = (-ADAM_LR) * ((nm * c1) / (jnp.sqrt(nv * c2) + ADAM_EPS) + ADAM_WD * w_ref[...])


def _adamw_small(params, name):
    n = len(params)

    def body(*refs):
        for k in range(n):
            _adamw_step(*refs[4 * k:4 * k + 4], *refs[4 * n + 3 * k:4 * n + 3 * k + 3])

    out = pl.pallas_call(
        body, name=name,
        out_shape=tuple(jax.ShapeDtypeStruct(p[0].shape, F32) for p in params for _ in range(3)),
    )(*[a for p in params for a in p])
    return [out[3 * k:3 * k + 3] for k in range(n)]


def _adamw(w, g, m, v, name, after=None):
    rows, cols = w.shape
    tr = rows
    while tr * cols * 4 > (1 << 20) and tr % 16 == 0:
        tr //= 2
    tokens = [] if after is None else [after]

    def body(*refs):
        _adamw_step(*refs[:4], *refs[4 + len(tokens):])

    spec = pl.BlockSpec((tr, cols), lambda i: (i, 0))
    shape = jax.ShapeDtypeStruct((rows, cols), F32)
    return pl.pallas_call(
        body, name=name,
        out_shape=(shape, shape, shape),
        grid=(rows // tr,),
        in_specs=[spec] * 4 + [ANY] * len(tokens), out_specs=(spec,) * 3,
        compiler_params=_params(dimension_semantics=("parallel",)),
    )(w, g, m, v, *tokens)


NAMES = ("ln1_g", "w_in", "b_in", "rpb", "w_att_o", "conv_w", "conv_b", "w_rg_a", "b_rg_a", "w_rg_i",
         "b_rg_i", "lru_lambda", "w_rec_o", "w_out", "ln2_g", "w_ff1", "w_ff2", "lnf_g")
TRANSPOSED = {"w_in": "w_in_t", "w_att_o": "w_att_o_t", "w_ff1": "w_ff1_t"}
ROW_SHARDED = ("w_rec_o", "w_out", "w_ff2")
REPLICATED = (("ln1_g", (1, D)), ("b_in", (1, D_IN)), ("rpb", (N_HEADS * N_RPB_R, N_RPB_C)),
              ("conv_b", (1, D_REC)), ("w_rg_a", (2 * N_REC_BLOCKS * REC_BLOCK, REC_BLOCK)),
              ("w_rg_i", (2 * N_REC_BLOCKS * REC_BLOCK, REC_BLOCK)), ("ln2_g", (1, D)), ("lnf_g", (1, D)))
GATE_BLOCKS = ("w_rg_a", "w_rg_i")
SMALL_ROWS = 112


def _chan_bits(vectors):
    chan = jnp.concatenate(vectors, axis=0)
    bits = lax.bitcast_convert_type(chan, BF16).reshape(-1)
    return jnp.pad(bits, (0, CHAN_BLOCK_ROWS * D - bits.shape[0])).reshape(CHAN_BLOCK_ROWS, D)


def _chan_from_bits(gathered):
    bits = gathered.reshape(N_DEV, CHAN_BLOCK_ROWS * D)[:, :2 * N_CHAN_ROWS * LANES]
    chan = lax.bitcast_convert_type(bits.reshape(N_DEV, N_CHAN_ROWS, LANES, 2), F32)
    return chan.transpose(1, 0, 2).reshape(N_CHAN_ROWS, D)


def kernel(x, ln1_g, w_in, b_in, rpb, w_att_o, conv_w, conv_b, w_rg_a, b_rg_a, w_rg_i, b_rg_i, lru_lambda, w_rec_o, w_out, ln2_g, w_ff1, w_ff2, lnf_g, loss_target, m_ln1_g, m_w_in, m_b_in, m_rpb, m_w_att_o, m_conv_w, m_conv_b, m_w_rg_a, m_b_rg_a, m_w_rg_i, m_b_rg_i, m_lru_lambda, m_w_rec_o, m_w_out, m_ln2_g, m_w_ff1, m_w_ff2, m_lnf_g, v_ln1_g, v_w_in, v_b_in, v_rpb, v_w_att_o, v_conv_w, v_conv_b, v_w_rg_a, v_b_rg_a, v_w_rg_i, v_b_rg_i, v_lru_lambda, v_w_rec_o, v_w_out, v_ln2_g, v_w_ff1, v_w_ff2, v_lnf_g):
    w = dict(zip(NAMES, (ln1_g, w_in, b_in, rpb, w_att_o, conv_w, conv_b, w_rg_a, b_rg_a, w_rg_i,
                         b_rg_i, lru_lambda, w_rec_o, w_out, ln2_g, w_ff1, w_ff2, lnf_g)))
    m = dict(zip(NAMES, (m_ln1_g, m_w_in, m_b_in, m_rpb, m_w_att_o, m_conv_w, m_conv_b, m_w_rg_a,
                         m_b_rg_a, m_w_rg_i, m_b_rg_i, m_lru_lambda, m_w_rec_o, m_w_out, m_ln2_g,
                         m_w_ff1, m_w_ff2, m_lnf_g)))
    v = dict(zip(NAMES, (v_ln1_g, v_w_in, v_b_in, v_rpb, v_w_att_o, v_conv_w, v_conv_b, v_w_rg_a,
                         v_b_rg_a, v_w_rg_i, v_b_rg_i, v_lru_lambda, v_w_rec_o, v_w_out, v_ln2_g,
                         v_w_ff1, v_w_ff2, v_lnf_g)))
    xi, yi, ci = _position()

    shard = {t: w[n][0].T.astype(BF16) for n, t in TRANSPOSED.items()}
    shard.update({n: w[n][0].astype(BF16) for n in ROW_SHARDED})
    shard["chan"] = _chan_bits([w[n][0] for n, _ in CHAN])
    first, later = ("w_in_t", "chan"), ("w_rec_o", "w_out", "w_att_o_t", "w_ff1_t", "w_ff2")
    *gathered, done = _all_gather([shard[n] for n in first], "weight_all_gather")
    p = dict(zip(first, gathered))
    send_sems, recv_sems, sent, zones, token = _gather_start([shard[n] for n in later], done,
                                                             "weight_gather_start")

    travelling = {"shards": sent, "zones": zones}
    stages = (("w_rec_o", "w_out", "w_att_o_t"), ("w_ff1_t", "w_ff2"))

    def late_weights(after, stage):
        which = [later.index(n) for n in stages[stage]]
        travelling["shards"], travelling["zones"] = _gather_wait(
            send_sems, recv_sems, travelling["shards"], travelling["zones"], which, after,
            "weight_gather_wait_%d" % stage)
        return dict(zip(stages[stage], _gather_pass_on(
            [shard[n].shape[0] for n in stages[stage]], [travelling["zones"][i] for i in which],
            PASS_ON_IDS[stage], "weight_gather_pass_on_%d" % stage)))

    chan = _chan_from_bits(p.pop("chan"))
    r0 = 0
    for n, rows in CHAN:
        p[n] = chan[r0:r0 + rows]
        r0 += rows
    p.update(ln1_g=w["ln1_g"], b_in=w["b_in"] + token[0, 0], rpb=w["rpb"][0], conv_b=w["conv_b"],
             w_rg_a=w["w_rg_a"][0], w_rg_i=w["w_rg_i"][0], ln2_g=w["ln2_g"],
             lnf_g=w["lnf_g"].reshape(1, D))

    core = jnp.reshape(ci, (1,)).astype(jnp.int32)
    chip = jnp.reshape(2 * xi + yi, (1,)).astype(jnp.int32)
    first_sections = tuple(s for s in SECTIONS if s[0] in ("w_ff1_t", "w_ff2"))
    late_sections = SECTIONS[:1]
    early_sections = tuple(s for s in SECTIONS[1:] if s not in first_sections)
    in_flight = {}

    def pair_sum_and_send(group, sections, after):
        send_sems, recv_sems, sect, zones, _ = in_flight["pair_" + group]
        sect, got = _pair_exchange_wait(sections, send_sems, recv_sems, sect, zones, after,
                                        "grad_pair_exchange_wait_" + group)
        parts = _pair_add(sections, sect, got, core, "grad_pair_add_" + group)
        in_flight[group] = _chip_exchange_start(sections, parts, "grad_chip_exchange_start_" + group)
        return in_flight[group][-1]

    def pair_exchange_at_once(group, sections, grads, barrier_id):
        in_flight["pair_" + group] = _pair_exchange_start(
            sections, [grads[n] for n, _, _ in sections], barrier_id, "grad_pair_exchange_start_" + group)
        return pair_sum_and_send(group, sections, in_flight["pair_" + group][-1])

    def reduce_first(grads, after):
        if grads is None:
            return pair_sum_and_send("first", first_sections, after)
        in_flight["pair_first"] = _pair_exchange_start(
            first_sections, [grads[n] for n, _, _ in first_sections], PAIR_FIRST_ID,
            "grad_pair_exchange_start_first")
        return in_flight["pair_first"][-1]

    def reduce_early(grads):
        chan_g = jnp.concatenate([grads[n] for n, _ in CHAN], axis=0)
        chan_g = chan_g.reshape(N_CHAN_ROWS, N_DEV, LANES).transpose(1, 0, 2).astype(BF16)
        chan_g = jnp.pad(chan_g.reshape(N_DEV, -1), ((0, 0), (0, CHAN_BLOCK_ROWS * D - N_CHAN_ROWS * LANES)))
        grads["chan"] = chan_g.reshape(N_DEV * CHAN_BLOCK_ROWS, D)
        grads["gates"] = jnp.concatenate([grads[n].reshape(-1, D) for n in GATE_BLOCKS], axis=0).astype(BF16)
        return pair_exchange_at_once("early", early_sections, grads, PAIR_EARLY_ID)[0, 0]

    def finish(group, sections, after, name):
        send_sems, recv_sems, parts, zones, _ = in_flight[group]
        parts, far = _chip_exchange_wait(sections, send_sems, recv_sems, parts, zones, after,
                                         "grad_chip_exchange_wait_" + name)
        return dict(zip((n for n, _, _ in sections),
                        _grad_finish(sections, parts, far, chip, "grad_finish_" + name)))

    summed = {}

    def reduce_late(grads):
        in_flight["pair_late"] = _pair_exchange_start(
            late_sections, [grads[n] for n, _, _ in late_sections], PAIR_LATE_ID,
            "grad_pair_exchange_start_late")
        summed.update(finish("first", first_sections, in_flight["pair_late"][-1], "first"))
        summed.update(finish("early", early_sections, summed["w_ff2"], "early"))
        return pair_sum_and_send("late", late_sections, summed["gates"])

    loss_part, grad_x, grads = _local_step(x[0], loss_target[0], p, late_weights, reduce_first, reduce_early,
                                           reduce_late)

    flat = jnp.concatenate([grads[n].reshape(-1) for n, _ in REPLICATED if n not in GATE_BLOCKS]
                           + [loss_part.reshape(-1)])
    n_small = flat.shape[0]
    flat = jnp.pad(flat, (0, SMALL_ROWS * LANES - n_small)).reshape(SMALL_ROWS, LANES)
    *small_gather, small_started = _gather_start([flat, summed["gates"]], None, "small_grad_gather_start")

    g, delta, new_m, new_v = {}, {}, {}, {}

    def update(n, g2, shape2, after=None):
        d2, m2, v2 = _adamw(w[n].reshape(shape2), g2, m[n].reshape(shape2), v[n].reshape(shape2),
                            "adamw_" + n, after)
        g[n], delta[n], new_m[n], new_v[n] = (a.reshape(w[n].shape) for a in (g2, d2, m2, v2))

    for n in ROW_SHARDED:
        update(n, summed[n], summed[n].shape, small_started)
    for n, t in TRANSPOSED.items():
        if t in summed:
            update(n, summed[t].T, summed[t].shape[::-1], small_started)

    small_sent, small_zones = _gather_wait(
        *small_gather, range(2), _after_all(list(delta.values()), "sharded_updates_done"),
        "small_grad_gather_wait")
    small_parts, gate_sum = _gather_pass_on([a.shape[0] for a in small_sent], small_zones, SMALL_PASS_ON_ID,
                                            "small_grad_gather_pass_on")
    small = _sum_devices(small_parts, SMALL_ROWS, "small_grad_sum").reshape(-1)
    loss = small[n_small - 1]

    small_params = []
    o = 0
    for n, shape2 in REPLICATED:
        if n in GATE_BLOCKS:
            k, rows = GATE_BLOCKS.index(n), gate_sum.shape[0] // len(GATE_BLOCKS)
            update(n, gate_sum[k * rows:(k + 1) * rows].reshape(shape2), shape2)
        else:
            size = shape2[0] * shape2[1]
            small_params.append((n, small[o:o + size].reshape(shape2), shape2))
            o += size
    chan_back = summed["chan"].reshape(-1)[:N_CHAN_ROWS * LANES].reshape(N_CHAN_ROWS, LANES)
    r0 = 0
    for n, rows in CHAN:
        small_params.append((n, chan_back[r0:r0 + rows], (rows, LANES)))
        r0 += rows
    results = _adamw_small([(w[n].reshape(s2), g2, m[n].reshape(s2), v[n].reshape(s2))
                            for n, g2, s2 in small_params], "adamw_vectors")
    for (n, g2, _), (d2, m2, v2) in zip(small_params, results):
        g[n], delta[n], new_m[n], new_v[n] = (a.reshape(w[n].shape) for a in (g2, d2, m2, v2))

    summed = finish("late", late_sections, _after_all(list(delta.values()), "updates_done"), "late")
    g_t = summed["w_in_t"]
    results = _adamw(w["w_in"][0].T, g_t, m["w_in"][0].T, v["w_in"][0].T, "adamw_w_in")
    g["w_in"], delta["w_in"], new_m["w_in"], new_v["w_in"] = (a.T[None] for a in (g_t, *results))

    return (loss, grad_x[None], *[g[n] for n in NAMES], *[delta[n] for n in NAMES],
            *[new_m[n] for n in NAMES], *[new_v[n] for n in NAMES])
```

```python
import math

import numpy as np
import jax
import jax.numpy as jnp
from jax import lax
from jax.experimental import pallas as pl
from jax.experimental.pallas import tpu as pltpu

F32 = jnp.float32
BF16 = jnp.bfloat16

T = 2048
D = 1024
D_ATT = 512
D_REC = 1024
D_FF = 4096
D_IN = 5632
N_HEADS = 8
DH = 64
GRID_W = 64
ROWS = T // GRID_W
WIN_H = 8
WIN_W = 16
KWIN = WIN_H * GRID_W
N_RPB_R = 2 * WIN_H - 1
N_RPB_C = 2 * WIN_W - 1
N_REC_BLOCKS = 16
REC_BLOCK = 64
CG = 128
N_CG = D_REC // CG
LRU_C = 8.0
EPS = 1e-6
N_DEV = 8
N_CHIPS = 4
LANES = 128

ADAM_LR = 0.001
ADAM_B1 = 0.9
ADAM_B2 = 0.999
ADAM_EPS = 1e-08
ADAM_WD = 0.01
ADAM_STEP = 10

MESH_AXES = ("x", "y", "c")
VMEM_LIMIT = 56 * 1024 * 1024

TILE = 512
DZ_ARRAYS = ((0, 3, 1), (3, 4, 2), (7, 4, 2))
N_DZ_TILES = D_IN // TILE


def _params(**kw):
    return pltpu.CompilerParams(vmem_limit_bytes=VMEM_LIMIT, **kw)


HG = 4
HQ = HG * GRID_W
HC = HG * DH


def _att_tables():
    rq = np.arange(GRID_W)
    kc = np.arange(KWIN) % GRID_W
    win_start = np.clip(rq - WIN_W // 2, 0, GRID_W - WIN_W)
    valid = (kc[None, :] >= win_start[:, None]) & (kc[None, :] < win_start[:, None] + WIN_W)
    same_head = (np.arange(HQ)[:, None] // GRID_W) == (np.arange(HC)[None, :] // DH)
    return valid.astype(np.float32), same_head.astype(np.float32)


def _pair_mask():
    half = np.arange(2 * DH) // DH
    return (half[:, None] == half[None, :]).astype(np.float32)


def _dup_table():
    return np.concatenate([np.eye(REC_BLOCK, dtype=np.float32)] * 2, axis=1)


def _sigmoid(x):
    return 0.5 * jnp.tanh(0.5 * x) + 0.5


def _softplus(x):
    return jnp.maximum(x, 0.0) + jnp.log(1.0 + jnp.exp(-jnp.abs(x)))


def _one_minus_square(log_a, a):
    x = 2.0 * log_a
    series = -x * (1.0 + x * (0.5 + x * (1.0 / 6.0)))
    return jnp.where(x > -0.02, series, 1.0 - a * a)


_GELU_C = math.sqrt(2.0 / math.pi)


def _gelu_and_grad(x):
    x2 = x * x
    inner = _GELU_C * (x + 0.044715 * x * x2)
    t = jnp.tanh(inner)
    g = 0.5 * x * (1.0 + t)
    dg = 0.5 * (1.0 + t) + 0.5 * x * (1.0 - t * t) * _GELU_C * (1.0 + 3.0 * 0.044715 * x2)
    return g, dg


def _dot(a, b):
    return jnp.dot(a, b, preferred_element_type=F32)


def _dot_nt(a, b):
    return lax.dot_general(a, b, (((1,), (1,)), ((), ())), preferred_element_type=F32)


def _dot_tn(a, b):
    return lax.dot_general(a, b, (((0,), (0,)), ((), ())), preferred_element_type=F32)


def _dot_exact(a, b):
    return jnp.dot(a, b, precision=lax.Precision.HIGHEST, preferred_element_type=F32)


def _shift_rows(x, s):
    n = x.shape[0]
    rows = lax.broadcasted_iota(jnp.int32, x.shape, 0)
    y = pltpu.roll(x, s % n, 0)
    if s > 0:
        return jnp.where(rows >= s, y, 0.0)
    return jnp.where(rows < n + s, y, 0.0)


def _rms_bwd(dh, xh, r, g):
    dxh = dh * g
    return r * (dxh - xh * jnp.mean(dxh * xh, axis=-1, keepdims=True))


def _matmul(a, b, mode, out_dtype, name, tm=512, tn=1024, tk=2048):
    if mode == "nn":
        (m, k), (k2, n) = a.shape, b.shape
    elif mode == "nt":
        (m, k), (n, k2) = a.shape, b.shape
    else:
        (k, m), (k2, n) = a.shape, b.shape
    assert k == k2
    tm, tn, tk = min(tm, m), min(tn, n), min(tk, k)
    assert m % tm == 0 and n % tn == 0 and k % tk == 0
    nk = k // tk
    dot = {"nn": _dot, "nt": _dot_nt, "tn": _dot_tn}[mode]

    def body(a_ref, b_ref, o_ref, acc):
        kk = pl.program_id(2)
        part = dot(a_ref[...].astype(BF16), b_ref[...].astype(BF16))
        if nk == 1:
            o_ref[...] = part.astype(out_dtype)
            return

        @pl.when(kk == 0)
        def _():
            acc[...] = part

        @pl.when(kk > 0)
        def _():
            acc[...] += part

        @pl.when(kk == nk - 1)
        def _():
            o_ref[...] = acc[...].astype(out_dtype)

    if mode == "tn":
        a_spec = pl.BlockSpec((tk, tm), lambda i, j, kk: (kk, i))
    else:
        a_spec = pl.BlockSpec((tm, tk), lambda i, j, kk: (i, kk))
    if mode == "nt":
        b_spec = pl.BlockSpec((tn, tk), lambda i, j, kk: (j, kk))
    else:
        b_spec = pl.BlockSpec((tk, tn), lambda i, j, kk: (kk, j))
    return pl.pallas_call(
        body, name=name,
        out_shape=jax.ShapeDtypeStruct((m, n), out_dtype),
        grid=(m // tm, n // tn, nk),
        in_specs=[a_spec, b_spec],
        out_specs=pl.BlockSpec((tm, tn), lambda i, j, kk: (i, j)),
        scratch_shapes=[pltpu.VMEM((tm, tn) if nk > 1 else (8, LANES), F32)],
        compiler_params=_params(dimension_semantics=("parallel", "parallel", "arbitrary")),
    )(a, b)


def _in_proj(x, g1, w_in_t, b_in):
    tm = 512

    def body(x_ref, g_ref, w_hbm, b_ref, qkv_ref, uy_ref, gg_ref, h_ref, w):
        @pl.when(pl.program_id(0) == 0)
        def _():
            pltpu.sync_copy(w_hbm, w)

        xv = x_ref[...]
        r = lax.rsqrt(jnp.mean(xv * xv, axis=-1, keepdims=True) + EPS)
        h = ((xv * r) * g_ref[...]).astype(BF16)
        h_ref[...] = h
        row0 = 0
        for ref in (qkv_ref, uy_ref, gg_ref):
            for c0 in range(0, ref.shape[1], TILE):
                z = _dot_nt(h, w[row0:row0 + TILE, :]) + b_ref[:, row0:row0 + TILE]
                ref[:, c0:c0 + TILE] = z.astype(ref.dtype)
                row0 += TILE

    tok = lambda width: pl.BlockSpec((tm, width), lambda i: (i, 0))
    return pl.pallas_call(
        body, name="in_proj",
        out_shape=(jax.ShapeDtypeStruct((T, 3 * D_ATT), BF16),
                   jax.ShapeDtypeStruct((T, 2 * D_REC), F32),
                   jax.ShapeDtypeStruct((T, 2 * D), F32),
                   jax.ShapeDtypeStruct((T, D), BF16)),
        grid=(T // tm,),
        in_specs=[tok(D), pl.BlockSpec((1, D), lambda i: (0, 0)), pl.BlockSpec(memory_space=pl.ANY),
                  pl.BlockSpec((1, D_IN), lambda i: (0, 0))],
        out_specs=(tok(3 * D_ATT), tok(2 * D_REC), tok(2 * D), tok(D)),
        scratch_shapes=[pltpu.VMEM((D_IN, D), BF16)],
        compiler_params=_params(dimension_semantics=("arbitrary",)),
    )(x, g1, w_in_t, b_in)


def _dz_specs(rows, tile_of, row_of):
    def spec(off, n, per_plane):
        def index(*ids):
            t = jnp.clip(tile_of(*ids) - off, 0, n - 1)
            return (t // per_plane, row_of(*ids), t % per_plane)
        return pl.BlockSpec((1, rows, TILE), index)
    return [spec(off, n, per) for off, n, per in DZ_ARRAYS]


def _dh_norm1_bwd(dz, w_in_t, x, g1, dx1, after):
    tm = 512

    def body(dqkv_ref, duy_ref, dgg_ref, w_hbm, x_ref, g_ref, dx1_ref, after_ref, gx_ref, dg_ref, w):
        @pl.when(pl.program_id(0) == 0)
        def _():
            pltpu.sync_copy(w_hbm, w)
            dg_ref[...] = jnp.zeros_like(dg_ref)

        dh, row0 = None, 0
        for ref in (dqkv_ref, duy_ref, dgg_ref):
            for plane in range(ref.shape[0]):
                cols = ref.shape[2]
                part = _dot(ref[plane], w[row0:row0 + cols, :])
                dh = part if dh is None else dh + part
                row0 += cols
        xv = x_ref[...]
        r = lax.rsqrt(jnp.mean(xv * xv, axis=-1, keepdims=True) + EPS)
        xh = xv * r
        dg_ref[...] += jnp.sum(dh * xh, axis=0, keepdims=True)
        gx_ref[...] = dx1_ref[...] + _rms_bwd(dh, xh, r, g_ref[...])

    tok = pl.BlockSpec((tm, D), lambda i: (i, 0))
    vec = pl.BlockSpec((1, D), lambda i: (0, 0))
    planes = lambda a: pl.BlockSpec((a.shape[0], tm, a.shape[2]), lambda i: (0, i, 0))
    return pl.pallas_call(
        body, name="dh_norm1_bwd",
        out_shape=(jax.ShapeDtypeStruct((T, D), F32), jax.ShapeDtypeStruct((1, D), F32)),
        grid=(T // tm,),
        in_specs=[planes(a) for a in dz] + [pl.BlockSpec(memory_space=pl.ANY), tok, vec, tok,
                                            pl.BlockSpec(memory_space=pl.ANY)],
        out_specs=(tok, vec),
        scratch_shapes=[pltpu.VMEM((D_IN, D), BF16)],
        compiler_params=_params(dimension_semantics=("arbitrary",)),
    )(*dz, w_in_t, x, g1, dx1, after)


def _grad_w_in(dz, h):
    def body(*refs):
        seg_refs = refs[:3]
        h_ref, gw_ref, gb_ref = refs[3:]
        j = pl.program_id(0)

        for s, (off, n, _) in enumerate(DZ_ARRAYS):
            @pl.when((j >= off) & (j < off + n))
            def _(s=s):
                a = seg_refs[s][0]
                gw_ref[...] = _dot_tn(a, h_ref[...]).astype(BF16)
                gb_ref[...] = jnp.sum(a.astype(F32), axis=0, keepdims=True)

    return pl.pallas_call(
        body, name="grad_w_in",
        out_shape=(jax.ShapeDtypeStruct((D_IN, D), BF16), jax.ShapeDtypeStruct((1, D_IN), F32)),
        grid=(N_DZ_TILES,),
        in_specs=_dz_specs(T, lambda j: j, lambda j: 0) + [pl.BlockSpec((T, D), lambda j: (0, 0))],
        out_specs=(pl.BlockSpec((TILE, D), lambda j: (j, 0)), pl.BlockSpec((1, TILE), lambda j: (0, j))),
        compiler_params=_params(dimension_semantics=("parallel",)),
    )(*dz, h)


def _rpb_rows(rpb):
    padded = jnp.pad(rpb, ((0, 0), (0, 0), (0, GRID_W - N_RPB_C)))
    rows = [padded[:, WIN_H - 1 - oi: 2 * WIN_H - 1 - oi].reshape(N_HEADS // HG, HG, KWIN)
            for oi in range(WIN_H)]
    return jnp.stack(rows, axis=0)


SKEW = KWIN - (WIN_W - 1)


MASKED = -1e30


def _bias_tiles(rows_ref, valid, bias_s):
    for oi in range(WIN_H):
        for hh in range(HG):
            row = jnp.broadcast_to(rows_ref[oi, 0, hh:hh + 1, :], (GRID_W, KWIN))
            tile = pltpu.roll(row, SKEW, 1, stride=1, stride_axis=0)
            bias_s[oi, hh * GRID_W:(hh + 1) * GRID_W, :] = jnp.where(valid, tile, MASKED)


def _bias_tile_grads(gb_s, flip, out_ref):
    for oi in range(WIN_H):
        for hh in range(HG):
            g = _dot_exact(flip, gb_s[oi, hh * GRID_W:(hh + 1) * GRID_W, :])
            back = pltpu.roll(g, KWIN - (GRID_W - WIN_W), 1, stride=1, stride_axis=0)
            out_ref[0, oi, hh:hh + 1, :] = jnp.sum(back, axis=0, keepdims=True)


def _rpb_fold(row_grads):
    g = row_grads.transpose(1, 0, 2, 3).reshape(WIN_H, N_HEADS, WIN_H, GRID_W)
    g = g.transpose(0, 2, 1, 3)

    def body(g_ref, o_ref):
        for dr in range(N_RPB_R):
            terms = [g_ref[oi, i] for oi in range(WIN_H) for i in range(WIN_H) if i - oi + WIN_H - 1 == dr]
            acc = terms[0]
            for term in terms[1:]:
                acc = acc + term
            o_ref[dr] = acc

    out = pl.pallas_call(
        body, name="rpb_fold",
        out_shape=jax.ShapeDtypeStruct((N_RPB_R, N_HEADS, GRID_W), F32),
    )(g)
    return out.transpose(1, 0, 2)[:, :, :N_RPB_C]


ATT_GROUPS = N_HEADS // HG
ATT_UNROLL = 8


def _stacked(rows64, same_head):
    return jnp.where(same_head, jnp.concatenate([rows64] * HG, axis=0), jnp.zeros((), BF16))


def _own_heads(stacked):
    head = lax.broadcasted_iota(jnp.int32, (GRID_W, HC), 1) // DH
    out = stacked[:GRID_W]
    for h in range(1, HG):
        out = jnp.where(head == h, stacked[h * GRID_W:(h + 1) * GRID_W], out)
    return out


def _att_scores(q_ref, k_ref, bias_ref, same_head, r):
    rs = jnp.clip(r - WIN_H // 2, 0, ROWS - WIN_H)
    oi = r - rs
    q0 = pl.multiple_of(r * GRID_W, GRID_W)
    k0 = pl.multiple_of(rs * GRID_W, GRID_W)
    q2 = _stacked(q_ref[pl.ds(q0, GRID_W), :] * (DH ** -0.5), same_head)
    kw = k_ref[pl.ds(k0, KWIN), :]
    s = _dot_nt(q2, kw) + bias_ref[oi]
    e = jnp.exp(s - jnp.max(s, axis=-1, keepdims=True))
    return e, 1.0 / jnp.sum(e, axis=-1, keepdims=True), q2, kw, q0, k0, oi


def _att_specs():
    col = lambda off: pl.BlockSpec((T, HC), lambda g: (0, g + off * ATT_GROUPS))
    tables = [pl.BlockSpec((WIN_H, 1, HG, KWIN), lambda g: (0, g, 0, 0)),
              pl.BlockSpec((GRID_W, KWIN), lambda g: (0, 0)),
              pl.BlockSpec((HQ, HC), lambda g: (0, 0))]
    return col, tables, pltpu.VMEM((WIN_H, HQ, KWIN), F32)


def _att_fwd(qkv, bias_rows):
    valid_np, same_head_np = _att_tables()

    def body(q_ref, k_ref, v_ref, rows_ref, valid_ref, head_ref, o_ref, bias_s):
        same_head = head_ref[...] > 0.5
        _bias_tiles(rows_ref, valid_ref[...] > 0.5, bias_s)

        def row(r, carry):
            e, rl, _, _, q0, k0, _ = _att_scores(q_ref, k_ref, bias_s, same_head, r)
            o2 = _dot((e * rl).astype(BF16), v_ref[pl.ds(k0, KWIN), :])
            o_ref[pl.ds(q0, GRID_W), :] = _own_heads(o2).astype(BF16)
            return carry

        lax.fori_loop(0, ROWS, row, 0, unroll=ATT_UNROLL)

    col, tables, tiles = _att_specs()
    return pl.pallas_call(
        body, name="att_fwd",
        out_shape=jax.ShapeDtypeStruct((T, D_ATT), BF16),
        grid=(ATT_GROUPS,),
        in_specs=[col(0), col(1), col(2)] + tables,
        out_specs=col(0),
        scratch_shapes=[tiles],
        compiler_params=_params(dimension_semantics=("parallel",)),
    )(qkv, qkv, qkv, bias_rows, jnp.asarray(valid_np), jnp.asarray(same_head_np))


def _att_bwd(qkv, bias_rows, datt, after):
    valid_np, same_head_np = _att_tables()

    def body(q_ref, k_ref, v_ref, do_ref, rows_ref, valid_ref, head_ref, flip_ref,
             dqkv_ref, grows_ref, dk_acc, dv_acc, bias_s, gb_s):
        same_head = head_ref[...] > 0.5
        dk_acc[...] = jnp.zeros_like(dk_acc)
        dv_acc[...] = jnp.zeros_like(dv_acc)
        gb_s[...] = jnp.zeros_like(gb_s)
        _bias_tiles(rows_ref, valid_ref[...] > 0.5, bias_s)

        def row(r, carry):
            e, rl, q2, kw, q0, k0, oi = _att_scores(q_ref, k_ref, bias_s, same_head, r)
            do2 = _stacked(do_ref[pl.ds(q0, GRID_W), :], same_head)
            vw = v_ref[pl.ds(k0, KWIN), :]
            p = e * rl
            dp = _dot_nt(do2, vw)
            ds = p * (dp - jnp.sum(dp * p, axis=-1, keepdims=True))
            p16 = p.astype(BF16)
            ds16 = ds.astype(BF16)
            dv_acc[pl.ds(k0, KWIN), :] += _dot_tn(p16, do2)
            dk_acc[pl.ds(k0, KWIN), :] += _dot_tn(ds16, q2)
            dq2 = _dot(ds16, kw) * (DH ** -0.5)
            dqkv_ref[0, pl.ds(q0, GRID_W), :] = _own_heads(dq2).astype(BF16)
            gb_s[oi] += ds
            return carry

        lax.fori_loop(0, ROWS, row, 0, unroll=ATT_UNROLL)
        dqkv_ref[1] = dk_acc[...].astype(BF16)
        dqkv_ref[2] = dv_acc[...].astype(BF16)
        _bias_tile_grads(gb_s, flip_ref[...], grows_ref)

    col, tables, tiles = _att_specs()
    return pl.pallas_call(
        body, name="att_bwd",
        out_shape=(jax.ShapeDtypeStruct((3, T, D_ATT), BF16),
                   jax.ShapeDtypeStruct((ATT_GROUPS, WIN_H, HG, KWIN), F32)),
        grid=(ATT_GROUPS,),
        in_specs=[col(0), col(1), col(2), col(0)] + tables + [pl.BlockSpec((GRID_W, GRID_W), lambda g: (0, 0))],
        out_specs=(pl.BlockSpec((3, T, HC), lambda g: (0, 0, g)),
                   pl.BlockSpec((1, WIN_H, HG, KWIN), lambda g: (g, 0, 0, 0))),
        scratch_shapes=[pltpu.VMEM((T, HC), F32), pltpu.VMEM((T, HC), F32), tiles, tiles],
        compiler_params=_params(dimension_semantics=("parallel",)),
    )(qkv, qkv, qkv, datt, bias_rows, jnp.asarray(valid_np) + after, jnp.asarray(same_head_np),
      jnp.asarray(np.eye(GRID_W, dtype=np.float32)[::-1].copy()))


def _conv_taps(up):
    return (_shift_rows(up, 2), _shift_rows(up, 1), up, _shift_rows(up, -1))


def _pair_block_diag(w_pair, dup, same_half):
    return jnp.where(same_half, _dot(w_pair.astype(BF16), dup), 0.0).astype(BF16)


def _gates(u, u16, wa, ba, wi, bi, lam):
    r = _sigmoid(_dot(u16, wa) + ba)
    ig = _sigmoid(_dot(u16, wi) + bi)
    sp = _softplus(-lam)
    log_a = (-LRU_C) * r * sp
    a = jnp.exp(log_a)
    mult2 = jnp.maximum(_one_minus_square(log_a, a), 0.0)
    return r, ig, sp, a, jnp.sqrt(mult2), mult2


SCAN_BLOCKS = 8


def _scans(jobs):
    c = jobs[0][0].shape[1]
    nblk = T // 8
    rows = lax.broadcasted_iota(jnp.int32, (8, c), 0)

    def block(a, b, reverse):
        for s in (1, 2, 4):
            if reverse:
                keep = rows < 8 - s
                a_s = jnp.where(keep, pltpu.roll(a, 8 - s, 0), 1.0)
                b_s = jnp.where(keep, pltpu.roll(b, 8 - s, 0), 0.0)
            else:
                keep = rows >= s
                a_s = jnp.where(keep, pltpu.roll(a, s, 0), 1.0)
                b_s = jnp.where(keep, pltpu.roll(b, s, 0), 0.0)
            b = a * b_s + b
            a = a * a_s
        return a, b

    def step(i, carry):
        out = []
        for (a_ref, b_ref, h_ref, reverse), h_prev in zip(jobs, carry):
            for u in range(SCAN_BLOCKS):
                blk = i * SCAN_BLOCKS + u
                if reverse:
                    blk = nblk - 1 - blk
                t0 = pl.multiple_of(blk * 8, 8)
                a, b = block(a_ref[pl.ds(t0, 8), :], b_ref[pl.ds(t0, 8), :], reverse)
                h = a * h_prev + b
                h_ref[pl.ds(t0, 8), :] = h
                h_prev = jnp.broadcast_to(h[0:1] if reverse else h[7:8], (8, c))
            out.append(h_prev)
        return tuple(out)

    lax.fori_loop(0, nblk // SCAN_BLOCKS, step, tuple(jnp.zeros((8, c), F32) for _ in jobs))


def _rec_specs():
    tok = lambda off: pl.BlockSpec((T, CG), lambda g: (0, g + off))
    per_ch = lambda rows: pl.BlockSpec((rows, CG), lambda g: (0, g))
    wspec = pl.BlockSpec((2, 1, CG, REC_BLOCK), lambda g: (0, g, 0, 0))
    const = lambda shape: pl.BlockSpec(shape, lambda g: (0, 0))
    return tok, per_ch, wspec, const


def _rec_fwd(uy, conv_w, conv_b, w_a, b_a, w_i, b_i, lam):
    tok, per_ch, wspec, const = _rec_specs()

    def body(up_ref, yb_ref, cw_ref, cb_ref, wa_ref, ba_ref, wi_ref, bi_ref, lam_ref, dup_ref, half_ref,
             hf_ref, hb_ref, yrec_ref, am_ref, bx_f, bx_b):
        dup = dup_ref[...]
        same_half = half_ref[...] > 0.5
        taps = _conv_taps(up_ref[...])
        u = cb_ref[...]
        for j in range(4):
            u = u + taps[j] * cw_ref[j:j + 1, :]
        u16 = u.astype(BF16)
        for d, bx_s in enumerate((bx_f, bx_b)):
            wa = _pair_block_diag(wa_ref[d, 0], dup, same_half)
            wi = _pair_block_diag(wi_ref[d, 0], dup, same_half)
            _, ig, _, a, mult, _ = _gates(u, u16, wa, ba_ref[d:d + 1, :], wi, bi_ref[d:d + 1, :],
                                       lam_ref[d:d + 1, :])
            am_ref[2 * d] = a
            am_ref[2 * d + 1] = mult
            bx_s[...] = mult * (ig * u)
        _scans([(am_ref.at[0], bx_f, hf_ref, False), (am_ref.at[2], bx_b, hb_ref, True)])
        gelu, _ = _gelu_and_grad(yb_ref[...])
        yrec_ref[...] = ((hf_ref[...] + hb_ref[...]) * gelu).astype(BF16)

    return pl.pallas_call(
        body, name="rec_fwd",
        out_shape=(jax.ShapeDtypeStruct((T, D_REC), F32), jax.ShapeDtypeStruct((T, D_REC), F32),
                   jax.ShapeDtypeStruct((T, D_REC), BF16), jax.ShapeDtypeStruct((4, T, D_REC), F32)),
        grid=(N_CG,),
        in_specs=[tok(0), tok(N_CG), per_ch(4), per_ch(1), wspec, per_ch(2), wspec, per_ch(2), per_ch(2),
                  const((REC_BLOCK, CG)), const((CG, CG))],
        out_specs=(tok(0), tok(0), tok(0), pl.BlockSpec((4, T, CG), lambda g: (0, 0, g))),
        scratch_shapes=[pltpu.VMEM((T, CG), F32)] * 2,
        compiler_params=_params(dimension_semantics=("parallel",)),
    )(uy, uy, conv_w, conv_b, w_a, b_a, w_i, b_i, lam,
      jnp.asarray(_dup_table(), BF16), jnp.asarray(_pair_mask()))


def _rec_bwd(uy, hf, hb, am, dyrec, conv_w, conv_b, w_a, b_a, w_i, b_i, lam):
    tok, per_ch, wspec, const = _rec_specs()

    def body(up_ref, yb_ref, hf_ref, hb_ref, am_ref, dy_ref, cw_ref, cb_ref, wa_ref, ba_ref, wi_ref, bi_ref,
             lam_ref, dup_ref, dupt_ref, half_ref,
             duy_ref, dcw_ref, dcb_ref, dwa_ref, dba_ref, dwi_ref, dbi_ref, dlam_ref,
             a_s0, a_s1, dh_s, g_s0, g_s1):
        dup = dup_ref[...]
        dup_t = dupt_ref[...]
        same_half = half_ref[...] > 0.5
        taps = _conv_taps(up_ref[...])
        u = cb_ref[...]
        for j in range(4):
            u = u + taps[j] * cw_ref[j:j + 1, :]
        u16 = u.astype(BF16)
        gelu, dgelu = _gelu_and_grad(yb_ref[...])
        dy = dy_ref[...]
        duy_ref[1] = (dy * (hf_ref[...] + hb_ref[...]) * dgelu).astype(BF16)
        dh_s[...] = dy * gelu
        a_s0[...] = _shift_rows(am_ref[0], -1)
        a_s1[...] = _shift_rows(am_ref[2], 1)
        _scans([(a_s0, dh_s, g_s0, True), (a_s1, dh_s, g_s1, False)])
        du = jnp.zeros((T, CG), F32)
        for d, g_s in enumerate((g_s0, g_s1)):
            reverse = d == 1
            wa = _pair_block_diag(wa_ref[d, 0], dup, same_half)
            wi = _pair_block_diag(wi_ref[d, 0], dup, same_half)
            lam_d = lam_ref[d:d + 1, :]
            r = _sigmoid(_dot(u16, wa) + ba_ref[d:d + 1, :])
            ig = _sigmoid(_dot(u16, wi) + bi_ref[d:d + 1, :])
            sp = _softplus(-lam_d)
            a, mult = am_ref[2 * d], am_ref[2 * d + 1]
            mult2 = mult * mult
            g = g_s[...]
            h_prev = _shift_rows(hb_ref[...], -1) if reverse else _shift_rows(hf_ref[...], 1)
            da = g * h_prev
            dmult = g * (ig * u)
            dig = g * mult * u
            du = du + g * mult * ig
            dmult_dlog = jnp.where(mult2 > 0.0, -(a * a) * lax.rsqrt(mult2), 0.0)
            dlog_a = da * a + dmult * dmult_dlog
            dr = dlog_a * ((-LRU_C) * sp)
            dsp = jnp.sum(dlog_a * ((-LRU_C) * r), axis=0, keepdims=True)
            dlam_ref[d:d + 1, :] = dsp * (-_sigmoid(-lam_d))
            dga = dr * r * (1.0 - r)
            dgi = dig * ig * (1.0 - ig)
            dga16 = dga.astype(BF16)
            dgi16 = dgi.astype(BF16)
            du = du + _dot_nt(dga16, wa) + _dot_nt(dgi16, wi)
            dwa_ref[d, 0] = _dot_exact(jnp.where(same_half, _dot_tn(u16, dga16), 0.0), dup_t)
            dwi_ref[d, 0] = _dot_exact(jnp.where(same_half, _dot_tn(u16, dgi16), 0.0), dup_t)
            dba_ref[d:d + 1, :] = jnp.sum(dga, axis=0, keepdims=True)
            dbi_ref[d:d + 1, :] = jnp.sum(dgi, axis=0, keepdims=True)
        dcb_ref[...] = jnp.sum(du, axis=0, keepdims=True)
        for j in range(4):
            dcw_ref[j:j + 1, :] = jnp.sum(du * taps[j], axis=0, keepdims=True)
        dup_in = (_shift_rows(du, -2) * cw_ref[0:1, :] + _shift_rows(du, -1) * cw_ref[1:2, :]
                  + du * cw_ref[2:3, :] + _shift_rows(du, 1) * cw_ref[3:4, :])
        duy_ref[0] = dup_in.astype(BF16)

    wshape = jax.ShapeDtypeStruct((2, N_CG, CG, REC_BLOCK), F32)
    vec = lambda rows: jax.ShapeDtypeStruct((rows, D_REC), F32)
    dup_np = _dup_table()
    return pl.pallas_call(
        body, name="rec_bwd",
        out_shape=(jax.ShapeDtypeStruct((2, T, D_REC), BF16),
                   vec(4), vec(1), wshape, vec(2), wshape, vec(2), vec(2)),
        grid=(N_CG,),
        in_specs=[tok(0), tok(N_CG), tok(0), tok(0), pl.BlockSpec((4, T, CG), lambda g: (0, 0, g)), tok(0),
                  per_ch(4), per_ch(1), wspec, per_ch(2), wspec, per_ch(2), per_ch(2),
                  const((REC_BLOCK, CG)), const((CG, REC_BLOCK)), const((CG, CG))],
        out_specs=(pl.BlockSpec((2, T, CG), lambda g: (0, 0, g)),
                   per_ch(4), per_ch(1), wspec, per_ch(2), wspec, per_ch(2), per_ch(2)),
        scratch_shapes=[pltpu.VMEM((T, CG), F32)] * 5,
        compiler_params=_params(dimension_semantics=("parallel",)),
    )(uy, uy, hf, hb, am, dyrec, conv_w, conv_b, w_a, b_a, w_i, b_i, lam,
      jnp.asarray(dup_np, BF16), jnp.asarray(dup_np.T.copy()), jnp.asarray(_pair_mask()))


TM_MIX = 256


def _mix_specs():
    tok = lambda width, blk=0: pl.BlockSpec((TM_MIX, width), lambda i: (i, blk))
    full = lambda shape: pl.BlockSpec(shape, lambda i: (0, 0))
    return tok, full


def _mix_fwd(x, att, yrec, gg, w_att_o_t, w_rec_o, w_out):
    tok, full = _mix_specs()

    def body(x_ref, att_ref, yr_ref, ga_ref, gr_ref, wao_ref, wro_ref, wo_ref, x1_ref, mixed_ref):
        y_att = _dot_nt(att_ref[...], wao_ref[...])
        y_rec = _dot(yr_ref[...], wro_ref[...])
        mixed = (_sigmoid(ga_ref[...]) * y_att + _sigmoid(gr_ref[...]) * y_rec).astype(BF16)
        mixed_ref[...] = mixed
        x1_ref[...] = x_ref[...] + _dot(mixed, wo_ref[...])

    return pl.pallas_call(
        body, name="mix_fwd",
        out_shape=(jax.ShapeDtypeStruct((T, D), F32), jax.ShapeDtypeStruct((T, D), BF16)),
        grid=(T // TM_MIX,),
        in_specs=[tok(D), tok(D_ATT), tok(D_REC), tok(D, 0), tok(D, 1),
                  full((D, D_ATT)), full((D_REC, D)), full((D, D))],
        out_specs=(tok(D), tok(D)),
        compiler_params=_params(dimension_semantics=("parallel",)),
    )(x, att, yrec, gg, gg, w_att_o_t, w_rec_o, w_out)


def _mix_bwd(dx1, att, yrec, gg, w_att_o_t, w_rec_o, w_out, after):
    tok, full = _mix_specs()

    def body(dx_ref, att_ref, yr_ref, ga_ref, gr_ref, wao_ref, wro_ref, wo_ref, after_ref,
             dgg_ref, dya_ref, dyr_ref, datt_ref, dyrp_ref):
        dmixed = _dot_nt(dx_ref[...].astype(BF16), wo_ref[...])
        y_att = _dot_nt(att_ref[...], wao_ref[...])
        y_rec = _dot(yr_ref[...], wro_ref[...])
        sa = _sigmoid(ga_ref[...])
        sr = _sigmoid(gr_ref[...])
        dgg_ref[0] = (dmixed * y_att * sa * (1.0 - sa)).astype(BF16)
        dgg_ref[1] = (dmixed * y_rec * sr * (1.0 - sr)).astype(BF16)
        dya = (dmixed * sa).astype(BF16)
        dyr = (dmixed * sr).astype(BF16)
        dya_ref[...] = dya
        dyr_ref[...] = dyr
        datt_ref[...] = _dot(dya, wao_ref[...]).astype(BF16)
        dyrp_ref[...] = _dot_nt(dyr, wro_ref[...])

    return pl.pallas_call(
        body, name="mix_bwd",
        out_shape=(jax.ShapeDtypeStruct((2, T, D), BF16),
                   jax.ShapeDtypeStruct((T, D), BF16), jax.ShapeDtypeStruct((T, D), BF16),
                   jax.ShapeDtypeStruct((T, D_ATT), BF16), jax.ShapeDtypeStruct((T, D_REC), F32)),
        grid=(T // TM_MIX,),
        in_specs=[tok(D), tok(D_ATT), tok(D_REC), tok(D, 0), tok(D, 1),
                  full((D, D_ATT)), full((D_REC, D)), full((D, D)), pl.BlockSpec(memory_space=pl.ANY)],
        out_specs=(pl.BlockSpec((2, TM_MIX, D), lambda i: (0, i, 0)),
                   tok(D), tok(D), tok(D_ATT), tok(D_REC)),
        compiler_params=_params(dimension_semantics=("parallel",)),
    )(dx1, att, yrec, gg, gg, w_att_o_t, w_rec_o, w_out, after)


TM_FFN = 256
FF_CHUNK = 1024


def _ffn_loss(x1, target, g2, gf, w_ff1_t, w_ff2):
    n_chunks = D_FF // FF_CHUNK

    def body(x1_ref, tg_ref, g2_ref, gf_ref, w1_hbm, w2_hbm,
             loss_ref, dx1_ref, h2_ref, act_ref, dpre_ref, dx2_ref, dg2_ref, dgf_ref,
             w1, w2, relu_s):
        i = pl.program_id(0)

        @pl.when(i == 0)
        def _():
            pltpu.sync_copy(w1_hbm, w1)
            pltpu.sync_copy(w2_hbm, w2)
            loss_ref[...] = jnp.zeros_like(loss_ref)
            dg2_ref[...] = jnp.zeros_like(dg2_ref)
            dgf_ref[...] = jnp.zeros_like(dgf_ref)

        x1v = x1_ref[...]
        r2 = lax.rsqrt(jnp.mean(x1v * x1v, axis=-1, keepdims=True) + EPS)
        xh2 = x1v * r2
        h2 = (xh2 * g2_ref[...]).astype(BF16)
        h2_ref[...] = h2
        x2 = x1v
        for c in range(n_chunks):
            ff = slice(c * FF_CHUNK, (c + 1) * FF_CHUNK)
            rl = jnp.maximum(_dot_nt(h2, w1[ff, :]), 0.0)
            relu_s[:, ff] = rl
            act = (rl * rl).astype(BF16)
            act_ref[:, ff] = act
            x2 = x2 + _dot(act, w2[ff, :])
        r3 = lax.rsqrt(jnp.mean(x2 * x2, axis=-1, keepdims=True) + EPS)
        xh3 = x2 * r3
        err = xh3 * gf_ref[...] - tg_ref[...]
        loss_ref[...] += 0.5 * jnp.sum(jnp.mean(err * err, axis=-1, keepdims=True))
        dy = err * (1.0 / D)
        dgf_ref[...] += jnp.sum(dy * xh3, axis=0, keepdims=True)
        dx2 = _rms_bwd(dy, xh3, r3, gf_ref[...])
        dx2_16 = dx2.astype(BF16)
        dx2_ref[...] = dx2_16
        dh2 = jnp.zeros((TM_FFN, D), F32)
        for c in range(n_chunks):
            ff = slice(c * FF_CHUNK, (c + 1) * FF_CHUNK)
            dpre = (_dot_nt(dx2_16, w2[ff, :]) * (2.0 * relu_s[:, ff])).astype(BF16)
            dpre_ref[:, ff] = dpre
            dh2 = dh2 + _dot(dpre, w1[ff, :])
        dg2_ref[...] += jnp.sum(dh2 * xh2, axis=0, keepdims=True)
        dx1_ref[...] = dx2 + _rms_bwd(dh2, xh2, r2, g2_ref[...])

    tok = lambda width: pl.BlockSpec((TM_FFN, width), lambda i: (i, 0))
    vec = pl.BlockSpec((1, D), lambda i: (0, 0))
    hbm = pl.BlockSpec(memory_space=pl.ANY)
    return pl.pallas_call(
        body, name="ffn_loss",
        out_shape=(jax.ShapeDtypeStruct((8, 128), F32), jax.ShapeDtypeStruct((T, D), F32),
                   jax.ShapeDtypeStruct((T, D), BF16), jax.ShapeDtypeStruct((T, D_FF), BF16),
                   jax.ShapeDtypeStruct((T, D_FF), BF16), jax.ShapeDtypeStruct((T, D), BF16),
                   jax.ShapeDtypeStruct((1, D), F32), jax.ShapeDtypeStruct((1, D), F32)),
        grid=(T // TM_FFN,),
        in_specs=[tok(D), tok(D), vec, vec, hbm, hbm],
        out_specs=(pl.BlockSpec((8, 128), lambda i: (0, 0)), tok(D), tok(D), tok(D_FF), tok(D_FF), tok(D),
                   vec, vec),
        scratch_shapes=[pltpu.VMEM((D_FF, D), BF16), pltpu.VMEM((D_FF, D), BF16),
                        pltpu.VMEM((TM_FFN, D_FF), F32)],
        compiler_params=_params(dimension_semantics=("arbitrary",)),
    )(x1, target, g2, gf, w_ff1_t, w_ff2)


def _local_step(x, target, p, late_weights, late_weights_ready, reduce_first, reduce_early, reduce_late):
    bias = _rpb_rows(p["rpb"])
    pairs = lambda w: w.reshape(2, N_CG, CG, REC_BLOCK)
    w_a, w_i = pairs(p["w_rg_a"]), pairs(p["w_rg_i"])
    rec_params = (p["conv_w"], p["conv_b"], w_a, p["b_rg_a"], w_i, p["b_rg_i"], p["lru_lambda"])

    qkv, uy, gg, h = _in_proj(x, p["ln1_g"], p["w_in_t"], p["b_in"])
    hf, hb, yrec, am = _rec_fwd(uy, *rec_params)
    bias = bias + late_weights(yrec, 0)[0, 0]
    att = _att_fwd(qkv, bias)
    p = {**p, **late_weights_ready(late_weights(att, 1), 0)}
    x1, mixed = _mix_fwd(x, att, yrec, gg, p["w_att_o_t"], p["w_rec_o"], p["w_out"])
    p = {**p, **late_weights_ready(x1, 1)}
    loss8, dx1, h2, act, dpre, dx2, g_ln2, g_lnf = _ffn_loss(
        x1, target, p["ln2_g"], p["lnf_g"], p["w_ff1_t"], p["w_ff2"])

    grads = {"ln2_g": g_ln2, "lnf_g": g_lnf,
             "w_ff1_t": _matmul(dpre, h2, "tn", BF16, "g_w_ff1"),
             "w_ff2": _matmul(act, dx2, "tn", BF16, "g_w_ff2")}
    dgg, dya, dyr, datt, dyrp = _mix_bwd(dx1, att, yrec, gg, p["w_att_o_t"], p["w_rec_o"], p["w_out"],
                                         reduce_first(grads, None))
    lam_after = rec_params[-1] + reduce_first(None, dgg)[0, 0]
    duy, g_cw, g_cb, g_wa, g_ba, g_wi, g_bi, g_lam = _rec_bwd(uy, hf, hb, am, dyrp, *rec_params[:-1], lam_after)
    blocks = lambda g: g.reshape(2, N_REC_BLOCKS, REC_BLOCK, REC_BLOCK)
    grads.update({
        "w_att_o_t": _matmul(dya, att, "tn", BF16, "g_w_att_o"),
        "conv_w": g_cw, "conv_b": g_cb, "w_rg_a": blocks(g_wa), "b_rg_a": g_ba,
        "w_rg_i": blocks(g_wi), "b_rg_i": g_bi, "lru_lambda": g_lam,
        "w_rec_o": _matmul(yrec, dyr, "tn", BF16, "g_w_rec_o"),
        "w_out": _matmul(mixed, dx1, "tn", BF16, "g_w_out"),
    })
    dqkv, gbias = _att_bwd(qkv, bias, datt, reduce_early(grads))
    dz = (dqkv, duy, dgg)
    g_w_in_t, g_b_in = _grad_w_in(dz, h)
    grads.update(w_in_t=g_w_in_t, b_in=g_b_in)
    grad_x, g_ln1 = _dh_norm1_bwd(dz, p["w_in_t"], x, p["ln1_g"], dx1, reduce_late(grads))
    grads.update(ln1_g=g_ln1, rpb=_rpb_fold(gbias))
    return loss8[0:1, 0:1], grad_x, grads


MESH_ID = pl.DeviceIdType.MESH
ANY = pl.BlockSpec(memory_space=pl.ANY)

CHAN_BLOCK_ROWS = 32
GATE_ROWS = 2 * 2 * N_REC_BLOCKS * REC_BLOCK * REC_BLOCK // (N_DEV * D)
SECTIONS = (("w_in_t", 704, D), ("w_rec_o", 128, D), ("w_out", 128, D), ("w_ff1_t", 512, D),
            ("w_ff2", 512, D), ("chan", CHAN_BLOCK_ROWS, D), ("w_att_o_t", 128, D_ATT),
            ("gates", GATE_ROWS, D))
N_SEC = len(SECTIONS)
N_CHAN_ROWS = 10
CHAN = (("conv_w", 4), ("b_rg_a", 2), ("b_rg_i", 2), ("lru_lambda", 2))


def _position():
    return lax.axis_index("x"), lax.axis_index("y"), lax.axis_index("c")


def _other_chips(x, y):
    return [(1 - x, y), (x, 1 - y), (1 - x, 1 - y)]


PASS_ON_IDS, PAIR_EARLY_ID, PAIR_LATE_ID, PAIR_FIRST_ID, SMALL_PASS_ON_ID = (1, 4), 2, 3, 5, 6


def _pair_handshake(x, y, c):
    barrier = pltpu.get_barrier_semaphore()
    pl.semaphore_signal(barrier, inc=1, device_id=(x, y, 1 - c), device_id_type=MESH_ID)
    pl.semaphore_wait(barrier, 1)


def _block_of(ref, dev, rows):
    return ref.at[pl.ds(pl.multiple_of(dev * rows, 16), rows)]


def _all_gather(shards, name):
    ns = len(shards)

    def body(*refs):
        x_refs, out_refs, done_ref = refs[:ns], refs[ns:2 * ns], refs[2 * ns]
        send_sems, recv_sems, local_sems = refs[2 * ns + 1:]
        done_ref[0, 0] = 0.0
        x, y, c = _position()
        me, sibling = (x, y, c), (x, y, 1 - c)
        x_nbr, y_nbr, diagonal = _other_chips(x, y)
        north = c == 1
        relay_from = (jnp.where(north, x_nbr[0], y_nbr[0]), jnp.where(north, x_nbr[1], y_nbr[1]))
        relay_to = (jnp.where(north, y_nbr[0], x_nbr[0]), jnp.where(north, y_nbr[1], x_nbr[1]))

        def rows(s, px, py, pc):
            return _block_of(out_refs[s], 4 * px + 2 * py + pc, shards[s].shape[0])

        def copy(k, s, block, to, from_shard=False):
            return pltpu.make_async_remote_copy(
                src_ref=x_refs[s] if from_shard else rows(s, *block), dst_ref=rows(s, *block),
                send_sem=send_sems.at[k * ns + s], recv_sem=recv_sems.at[k * ns + s],
                device_id=to, device_id_type=MESH_ID)

        sections = range(ns)
        mine = [pltpu.make_async_copy(x_refs[s], rows(s, *me), local_sems.at[s]) for s in sections]
        sent = [copy(k, s, me, to, True) for k, to in enumerate((sibling, (*x_nbr, c), (*y_nbr, c)))
                for s in sections]
        for cp in mine + sent:
            cp.start()
        for s in sections:
            copy(1, s, (*x_nbr, c), me).wait_recv()
            copy(2, s, (*y_nbr, c), me).wait_recv()
            sent += [copy(3, s, (*relay_from, c), (*relay_to, c)),
                     copy(4, s, (*x_nbr, c), sibling), copy(5, s, (*y_nbr, c), sibling)]
            for cp in sent[-3:]:
                cp.start()
        for s in sections:
            copy(3, s, (*diagonal, c), me).wait_recv()
            sent.append(copy(6, s, (*diagonal, c), sibling))
            sent[-1].start()
        for s in sections:
            copy(0, s, sibling, me).wait_recv()
            for k, chip in ((4, x_nbr), (5, y_nbr), (6, diagonal)):
                copy(k, s, (*chip, 1 - c), me).wait_recv()
        for cp in sent:
            cp.wait_send()
        for cp in mine:
            cp.wait()

    return pl.pallas_call(
        body, name=name,
        out_shape=tuple(jax.ShapeDtypeStruct((N_DEV * s.shape[0], s.shape[1]), s.dtype) for s in shards)
        + (jax.ShapeDtypeStruct((1, 1), F32),),
        in_specs=[ANY] * ns,
        out_specs=(ANY,) * ns + (pl.BlockSpec(memory_space=pltpu.SMEM),),
        scratch_shapes=[pltpu.SemaphoreType.DMA((7 * ns,)), pltpu.SemaphoreType.DMA((7 * ns,)),
                        pltpu.SemaphoreType.DMA((ns,))],
    )(*shards)


HBM = pl.BlockSpec(memory_space=pltpu.HBM)
SEM = pl.BlockSpec(memory_space=pltpu.SEMAPHORE)
EFFECT = pltpu.SideEffectType.DATAFLOW_SIDE_EFFECTING


def _in_hbm(a):
    return pltpu.with_memory_space_constraint(a, pltpu.HBM)


def _first_hop_copies(shards, x_refs, zones, send_sems, recv_sems):
    ns = len(shards)
    x, y, c = _position()
    targets = [(x, y, 1 - c)] + [(cx, cy, c) for cx, cy in _other_chips(x, y)]
    return [pltpu.make_async_remote_copy(
        src_ref=x_refs[s], dst_ref=_block_of(zones[s], 4 * x + 2 * y + c, shards[s].shape[0]),
        send_sem=send_sems.at[k * ns + s], recv_sem=recv_sems.at[k * ns + s],
        device_id=to, device_id_type=MESH_ID)
        for k, to in enumerate(targets) for s in range(ns)]


def _after_all(arrays, name):
    def body(*refs):
        refs[-1][...] = jnp.zeros_like(refs[-1])

    return pl.pallas_call(
        body, name=name,
        out_shape=jax.ShapeDtypeStruct((8, LANES), F32),
        in_specs=[pl.BlockSpec(memory_space=pl.ANY)] * len(arrays),
        out_specs=pl.BlockSpec(memory_space=pltpu.VMEM),
    )(*arrays)


def _own_blocks_placed(shards, after, name):
    ns = len(shards)
    x, y, c = _position()
    me = jnp.reshape(4 * x + 2 * y + c, (1,)).astype(jnp.int32)
    if after is not None:
        shards = [*shards[:-1], shards[-1] + after.astype(shards[-1].dtype)]

    def body(me_ref, *refs):
        for s in range(ns):
            refs[ns + s][...] = refs[s][...]

    return pl.pallas_call(
        body, name=name,
        out_shape=tuple(jax.ShapeDtypeStruct((N_DEV * s.shape[0], s.shape[1]), s.dtype) for s in shards),
        grid_spec=pltpu.PrefetchScalarGridSpec(
            num_scalar_prefetch=1, grid=(1,),
            in_specs=[pl.BlockSpec(s.shape, lambda i, me: (0, 0)) for s in shards],
            out_specs=tuple(pl.BlockSpec(s.shape, lambda i, me: (me[0], 0)) for s in shards)),
        compiler_params=_params(dimension_semantics=("arbitrary",)),
    )(me, *shards)


def _gather_start(shards, after, name):
    ns = len(shards)
    zones = _own_blocks_placed(shards, after, name + "_own_blocks")

    def body(*refs):
        for cp in _first_hop_copies(shards, refs[:ns], refs[ns:2 * ns], refs[2 * ns], refs[2 * ns + 1]):
            cp.start()
        refs[-1][...] = jnp.zeros_like(refs[-1])

    out = pl.pallas_call(
        body, name=name,
        out_shape=(pltpu.SemaphoreType.DMA((4 * ns,)), pltpu.SemaphoreType.DMA((4 * ns,)),
                   *[pltpu.HBM(a.shape, a.dtype) for a in (*shards, *zones)],
                   jax.ShapeDtypeStruct((8, LANES), F32)),
        in_specs=[HBM] * (2 * ns),
        out_specs=(SEM, SEM, *[HBM] * (2 * ns), pl.BlockSpec(memory_space=pltpu.VMEM)),
        input_output_aliases={i: 2 + i for i in range(2 * ns)},
        compiler_params=pltpu.CompilerParams(has_side_effects=EFFECT),
    )(*[_in_hbm(a) for a in shards], *[_in_hbm(a) for a in zones])
    return out[0], out[1], out[2:2 + ns], out[2 + ns:2 + 2 * ns], out[-1]


def _gather_wait(send_sems, recv_sems, shards, zones, which, after, name):
    ns = len(shards)

    def body(*refs):
        copies = _first_hop_copies(shards, refs[:ns], refs[ns:2 * ns], refs[2 * ns], refs[2 * ns + 1])
        for i, cp in enumerate(copies):
            if i % ns in which:
                cp.wait_send()
                cp.wait_recv()

    out = pl.pallas_call(
        body, name=name,
        out_shape=tuple(pltpu.HBM(a.shape, a.dtype) for a in (*shards, *zones)),
        in_specs=[HBM] * (2 * ns) + [SEM, SEM, ANY],
        out_specs=(HBM,) * (2 * ns),
        input_output_aliases={i: i for i in range(2 * ns)},
        compiler_params=pltpu.CompilerParams(has_side_effects=EFFECT),
    )(*shards, *zones, send_sems, recv_sems, after)
    return out[:ns], out[ns:]


def _pass_on_copies(rows, in_refs, out_refs, send_sems, recv_sems):
    ns = len(rows)
    x, y, c = _position()
    return [pltpu.make_async_remote_copy(
        src_ref=_block_of(in_refs[s], 4 * cx + 2 * cy + c, rows[s]),
        dst_ref=_block_of(out_refs[s], 4 * cx + 2 * cy + c, rows[s]),
        send_sem=send_sems.at[j * ns + s], recv_sem=recv_sems.at[j * ns + s],
        device_id=(x, y, 1 - c), device_id_type=MESH_ID)
        for j, (cx, cy) in enumerate(_other_chips(x, y)) for s in range(ns)]


def _pass_on_start(rows, zones, barrier_id, name):
    ns = len(zones)

    def body(*refs):
        _pair_handshake(*_position())
        for cp in _pass_on_copies(rows, refs[:ns], refs[:ns], refs[ns], refs[ns + 1]):
            cp.start()
        refs[-1][...] = jnp.zeros_like(refs[-1])

    out = pl.pallas_call(
        body, name=name,
        out_shape=(pltpu.SemaphoreType.DMA((3 * ns,)), pltpu.SemaphoreType.DMA((3 * ns,)),
                   *[pltpu.HBM(z.shape, z.dtype) for z in zones], jax.ShapeDtypeStruct((8, LANES), F32)),
        in_specs=[HBM] * ns,
        out_specs=(SEM, SEM, *[HBM] * ns, pl.BlockSpec(memory_space=pltpu.VMEM)),
        input_output_aliases={i: 2 + i for i in range(ns)},
        compiler_params=pltpu.CompilerParams(has_side_effects=EFFECT, collective_id=barrier_id),
    )(*[_in_hbm(z) for z in zones])
    return out[0], out[1], out[2:2 + ns], out[-1]


def _pass_on_wait(rows, send_sems, recv_sems, zones, after, name):
    ns = len(zones)

    def body(*refs):
        for cp in _pass_on_copies(rows, refs[:ns], refs[:ns], refs[ns], refs[ns + 1]):
            cp.wait_send()
            cp.wait_recv()

    return pl.pallas_call(
        body, name=name,
        out_shape=tuple(pltpu.HBM(z.shape, z.dtype) for z in zones),
        in_specs=[HBM] * ns + [SEM, SEM, ANY],
        out_specs=(HBM,) * ns,
        input_output_aliases={i: i for i in range(ns)},
        compiler_params=pltpu.CompilerParams(has_side_effects=EFFECT),
    )(*zones, send_sems, recv_sems, after)


def _gather_pass_on(rows, zones, barrier_id, name):
    ns = len(zones)

    def body(*refs):
        _pair_handshake(*_position())
        copies = _pass_on_copies(rows, refs[:ns], refs[ns:2 * ns], *refs[2 * ns:])
        for cp in copies:
            cp.start()
        for cp in copies:
            cp.wait_recv()
        for cp in copies:
            cp.wait_send()

    return pl.pallas_call(
        body, name=name,
        out_shape=tuple(jax.ShapeDtypeStruct(z.shape, z.dtype) for z in zones),
        in_specs=[ANY] * ns, out_specs=(ANY,) * ns,
        input_output_aliases={i: i for i in range(ns)},
        scratch_shapes=[pltpu.SemaphoreType.DMA((3 * ns,)), pltpu.SemaphoreType.DMA((3 * ns,))],
        compiler_params=pltpu.CompilerParams(collective_id=barrier_id),
    )(*zones)


def _pair_copies(sections, g_refs, land, send_sems, recv_sems):
    ns = len(sections)
    x, y, c = _position()
    return [pltpu.make_async_remote_copy(
        src_ref=_block_of(g_refs[s], 2 * k + 1 - c, rows), dst_ref=land[s].at[k],
        send_sem=send_sems.at[k * ns + s], recv_sem=recv_sems.at[k * ns + s],
        device_id=(x, y, 1 - c), device_id_type=MESH_ID)
        for k in range(N_CHIPS) for s, (_, rows, _) in enumerate(sections)]


def _pair_exchange_start(sections, grads, barrier_id, name):
    ns = len(sections)

    def body(*refs):
        _pair_handshake(*_position())
        for cp in _pair_copies(sections, refs[:ns], refs[ns:2 * ns], refs[2 * ns], refs[2 * ns + 1]):
            cp.start()
        refs[-1][...] = jnp.zeros_like(refs[-1])

    zones = [lax.empty((N_CHIPS, rows, cols), BF16) for _, rows, cols in sections]
    n = N_CHIPS * ns
    out = pl.pallas_call(
        body, name=name,
        out_shape=(pltpu.SemaphoreType.DMA((n,)), pltpu.SemaphoreType.DMA((n,)),
                   *[pltpu.HBM(a.shape, a.dtype) for a in (*grads, *zones)],
                   jax.ShapeDtypeStruct((8, LANES), F32)),
        in_specs=[HBM] * (2 * ns),
        out_specs=(SEM, SEM, *[HBM] * (2 * ns), pl.BlockSpec(memory_space=pltpu.VMEM)),
        input_output_aliases={i: 2 + i for i in range(2 * ns)},
        compiler_params=pltpu.CompilerParams(has_side_effects=EFFECT, collective_id=barrier_id),
    )(*[_in_hbm(a) for a in grads], *[_in_hbm(a) for a in zones])
    return out[0], out[1], out[2:2 + ns], out[2 + ns:2 + 2 * ns], out[-1]


def _pair_exchange_wait(sections, send_sems, recv_sems, grads, zones, after, name):
    ns = len(sections)

    def body(*refs):
        for cp in _pair_copies(sections, refs[:ns], refs[ns:2 * ns], refs[2 * ns], refs[2 * ns + 1]):
            cp.wait_send()
            cp.wait_recv()

    out = pl.pallas_call(
        body, name=name,
        out_shape=tuple(pltpu.HBM(a.shape, a.dtype) for a in (*grads, *zones)),
        in_specs=[HBM] * (2 * ns) + [SEM, SEM, ANY],
        out_specs=(HBM,) * (2 * ns),
        input_output_aliases={i: i for i in range(2 * ns)},
        compiler_params=pltpu.CompilerParams(has_side_effects=EFFECT),
    )(*grads, *zones, send_sems, recv_sems, after)
    return out[:ns], out[ns:]


def _pair_add(sections, grads, got, core, name):
    ns = len(sections)

    def body(core_ref, *refs):
        g_refs, got_refs, p_refs = refs[:ns], refs[ns:2 * ns], refs[2 * ns:]
        for s in range(ns):
            p_refs[s][0] = (g_refs[s][...].astype(F32) + got_refs[s][0].astype(F32)).astype(BF16)

    slot = [pl.BlockSpec((1, rows, cols), lambda k, c: (k, 0, 0)) for _, rows, cols in sections]
    return pl.pallas_call(
        body, name=name,
        out_shape=tuple(jax.ShapeDtypeStruct((N_CHIPS, rows, cols), BF16) for _, rows, cols in sections),
        grid_spec=pltpu.PrefetchScalarGridSpec(
            num_scalar_prefetch=1, grid=(N_CHIPS,),
            in_specs=[pl.BlockSpec((rows, cols), lambda k, c: (2 * k + c[0], 0)) for _, rows, cols in sections]
            + slot,
            out_specs=tuple(slot)),
        compiler_params=_params(dimension_semantics=("parallel",)),
    )(core, *grads, *got)


def _chip_copies(sections, p_refs, land, send_sems, recv_sems):
    ns = len(sections)
    x, y, c = _position()
    return [pltpu.make_async_remote_copy(
        src_ref=p_refs[s].at[2 * cx + cy], dst_ref=land[s].at[j],
        send_sem=send_sems.at[j * ns + s], recv_sem=recv_sems.at[j * ns + s],
        device_id=(cx, cy, c), device_id_type=MESH_ID)
        for j, (cx, cy) in enumerate(_other_chips(x, y)) for s in range(ns)]


def _chip_exchange(sections, parts, name):
    ns = len(sections)

    def body(*refs):
        copies = _chip_copies(sections, refs[:ns], refs[ns:2 * ns], *refs[2 * ns:])
        for cp in copies:
            cp.start()
        for cp in copies:
            cp.wait_recv()
        for cp in copies:
            cp.wait_send()

    n = 3 * ns
    return pl.pallas_call(
        body, name=name,
        out_shape=tuple(jax.ShapeDtypeStruct((3, rows, cols), BF16) for _, rows, cols in sections),
        in_specs=[ANY] * ns, out_specs=(ANY,) * ns,
        scratch_shapes=[pltpu.SemaphoreType.DMA((n,)), pltpu.SemaphoreType.DMA((n,))],
    )(*parts)


def _chip_exchange_start(sections, parts, name):
    ns = len(sections)

    def body(*refs):
        p_refs, land = refs[:ns], refs[ns:2 * ns]
        send_sems, recv_sems = refs[2 * ns], refs[2 * ns + 1]
        token = refs[-1]
        for cp in _chip_copies(sections, p_refs, land, send_sems, recv_sems):
            cp.start()
        token[...] = jnp.zeros_like(token)

    zones = [lax.empty((3, rows, cols), BF16) for _, rows, cols in sections]
    out = pl.pallas_call(
        body, name=name,
        out_shape=(pltpu.SemaphoreType.DMA((3 * ns,)), pltpu.SemaphoreType.DMA((3 * ns,)),
                   *[pltpu.HBM(a.shape, a.dtype) for a in parts], *[pltpu.HBM(a.shape, a.dtype) for a in zones],
                   jax.ShapeDtypeStruct((8, LANES), F32)),
        in_specs=[HBM] * (2 * ns),
        out_specs=(SEM, SEM, *[HBM] * (2 * ns), pl.BlockSpec(memory_space=pltpu.VMEM)),
        input_output_aliases={i: 2 + i for i in range(2 * ns)},
        compiler_params=pltpu.CompilerParams(has_side_effects=EFFECT),
    )(*[_in_hbm(a) for a in parts], *[_in_hbm(a) for a in zones])
    return out[0], out[1], out[2:2 + ns], out[2 + ns:2 + 2 * ns], out[-1]


def _chip_exchange_wait(sections, send_sems, recv_sems, parts, zones, after, name):
    ns = len(sections)

    def body(*refs):
        p_refs, land = refs[:ns], refs[ns:2 * ns]
        for cp in _chip_copies(sections, p_refs, land, refs[2 * ns], refs[2 * ns + 1]):
            cp.wait_send()
            cp.wait_recv()

    out = pl.pallas_call(
        body, name=name,
        out_shape=tuple(pltpu.HBM(a.shape, a.dtype) for a in (*parts, *zones)),
        in_specs=[HBM] * (2 * ns) + [SEM, SEM, ANY],
        out_specs=(HBM,) * (2 * ns),
        input_output_aliases={i: i for i in range(2 * ns)},
        compiler_params=pltpu.CompilerParams(has_side_effects=EFFECT),
    )(*parts, *zones, send_sems, recv_sems, after)
    return out[:ns], out[ns:]


def _grad_finish(sections, parts, far, chip, name):
    ns = len(sections)

    def body(chip_ref, *refs):
        p_refs, b_refs, g_refs = refs[:ns], refs[ns:2 * ns], refs[2 * ns:]
        for s in range(ns):
            g = p_refs[s][0].astype(F32)
            for j in range(3):
                g = g + b_refs[s][j].astype(F32)
            g_refs[s][...] = g

    half = [(rows // 2, cols) for _, rows, cols in sections]
    return pl.pallas_call(
        body, name=name,
        out_shape=tuple(jax.ShapeDtypeStruct((rows, cols), F32) for _, rows, cols in sections),
        grid_spec=pltpu.PrefetchScalarGridSpec(
            num_scalar_prefetch=1, grid=(2,),
            in_specs=[pl.BlockSpec((1, r, c), lambda i, chip: (chip[0], i, 0)) for r, c in half]
            + [pl.BlockSpec((3, r, c), lambda i, chip: (0, i, 0)) for r, c in half],
            out_specs=tuple(pl.BlockSpec((r, c), lambda i, chip: (i, 0)) for r, c in half)),
        compiler_params=_params(dimension_semantics=("parallel",)),
    )(chip, *parts, *far)


def _sum_devices(parts, rows, name):
    cols = parts.shape[1]
    tr = rows // 2

    def body(*refs):
        s = refs[0][...].astype(F32)
        for d in range(1, N_DEV):
            s = s + refs[d][...].astype(F32)
        refs[N_DEV][...] = s

    return pl.pallas_call(
        body, name=name,
        out_shape=jax.ShapeDtypeStruct((rows, cols), F32),
        grid=(2,),
        in_specs=[pl.BlockSpec((tr, cols), lambda i, d=d: (2 * d + i, 0)) for d in range(N_DEV)],
        out_specs=pl.BlockSpec((tr, cols), lambda i: (i, 0)),
        compiler_params=_params(dimension_semantics=("parallel",)),
    )(*([parts] * N_DEV))


def _adamw_step(w_ref, g_ref, m_ref, v_ref, d_ref, nm_ref, nv_ref):
    c1 = 1.0 / (1.0 - ADAM_B1 ** ADAM_STEP)
    c2 = 1.0 / (1.0 - ADAM_B2 ** ADAM_STEP)
    gv = g_ref[...]
    nm = ADAM_B1 * m_ref[...] + (1.0 - ADAM_B1) * gv
    nv = ADAM_B2 * v_ref[...] + (1.0 - ADAM_B2) * (gv * gv)
    nm_ref[...] = nm
    nv_ref[...] = nv
    d_ref[...] = (-ADAM_LR) * ((nm * c1) / (jnp.sqrt(nv * c2) + ADAM_EPS) + ADAM_WD * w_ref[...])


def _adamw_small(params, name):
    n = len(params)

    def body(*refs):
        for k in range(n):
            _adamw_step(*refs[4 * k:4 * k + 4], *refs[4 * n + 3 * k:4 * n + 3 * k + 3])

    out = pl.pallas_call(
        body, name=name,
        out_shape=tuple(jax.ShapeDtypeStruct(p[0].shape, F32) for p in params for _ in range(3)),
    )(*[a for p in params for a in p])
    return [out[3 * k:3 * k + 3] for k in range(n)]


def _adamw(w, g, m, v, name, after=None):
    rows, cols = w.shape
    tr = rows
    while tr * cols * 4 > (1 << 20) and tr % 16 == 0:
        tr //= 2
    tokens = [] if after is None else [after]

    def body(*refs):
        _adamw_step(*refs[:4], *refs[4 + len(tokens):])

    spec = pl.BlockSpec((tr, cols), lambda i: (i, 0))
    shape = jax.ShapeDtypeStruct((rows, cols), F32)
    return pl.pallas_call(
        body, name=name,
        out_shape=(shape, shape, shape),
        grid=(rows // tr,),
        in_specs=[spec] * 4 + [ANY] * len(tokens), out_specs=(spec,) * 3,
        compiler_params=_params(dimension_semantics=("parallel",)),
    )(w, g, m, v, *tokens)


NAMES = ("ln1_g", "w_in", "b_in", "rpb", "w_att_o", "conv_w", "conv_b", "w_rg_a", "b_rg_a", "w_rg_i",
         "b_rg_i", "lru_lambda", "w_rec_o", "w_out", "ln2_g", "w_ff1", "w_ff2", "lnf_g")
TRANSPOSED = {"w_in": "w_in_t", "w_att_o": "w_att_o_t", "w_ff1": "w_ff1_t"}
ROW_SHARDED = ("w_rec_o", "w_out", "w_ff2")
REPLICATED = (("ln1_g", (1, D)), ("b_in", (1, D_IN)), ("rpb", (N_HEADS * N_RPB_R, N_RPB_C)),
              ("conv_b", (1, D_REC)), ("w_rg_a", (2 * N_REC_BLOCKS * REC_BLOCK, REC_BLOCK)),
              ("w_rg_i", (2 * N_REC_BLOCKS * REC_BLOCK, REC_BLOCK)), ("ln2_g", (1, D)), ("lnf_g", (1, D)))
GATE_BLOCKS = ("w_rg_a", "w_rg_i")
SMALL_ROWS = 112


def _chan_bits(vectors):
    chan = jnp.concatenate(vectors, axis=0)
    bits = lax.bitcast_convert_type(chan, BF16).reshape(-1)
    return jnp.pad(bits, (0, CHAN_BLOCK_ROWS * D - bits.shape[0])).reshape(CHAN_BLOCK_ROWS, D)


def _chan_from_bits(gathered):
    bits = gathered.reshape(N_DEV, CHAN_BLOCK_ROWS * D)[:, :2 * N_CHAN_ROWS * LANES]
    chan = lax.bitcast_convert_type(bits.reshape(N_DEV, N_CHAN_ROWS, LANES, 2), F32)
    return chan.transpose(1, 0, 2).reshape(N_CHAN_ROWS, D)


def kernel(x, ln1_g, w_in, b_in, rpb, w_att_o, conv_w, conv_b, w_rg_a, b_rg_a, w_rg_i, b_rg_i, lru_lambda, w_rec_o, w_out, ln2_g, w_ff1, w_ff2, lnf_g, loss_target, m_ln1_g, m_w_in, m_b_in, m_rpb, m_w_att_o, m_conv_w, m_conv_b, m_w_rg_a, m_b_rg_a, m_w_rg_i, m_b_rg_i, m_lru_lambda, m_w_rec_o, m_w_out, m_ln2_g, m_w_ff1, m_w_ff2, m_lnf_g, v_ln1_g, v_w_in, v_b_in, v_rpb, v_w_att_o, v_conv_w, v_conv_b, v_w_rg_a, v_b_rg_a, v_w_rg_i, v_b_rg_i, v_lru_lambda, v_w_rec_o, v_w_out, v_ln2_g, v_w_ff1, v_w_ff2, v_lnf_g):
    w = dict(zip(NAMES, (ln1_g, w_in, b_in, rpb, w_att_o, conv_w, conv_b, w_rg_a, b_rg_a, w_rg_i,
                         b_rg_i, lru_lambda, w_rec_o, w_out, ln2_g, w_ff1, w_ff2, lnf_g)))
    m = dict(zip(NAMES, (m_ln1_g, m_w_in, m_b_in, m_rpb, m_w_att_o, m_conv_w, m_conv_b, m_w_rg_a,
                         m_b_rg_a, m_w_rg_i, m_b_rg_i, m_lru_lambda, m_w_rec_o, m_w_out, m_ln2_g,
                         m_w_ff1, m_w_ff2, m_lnf_g)))
    v = dict(zip(NAMES, (v_ln1_g, v_w_in, v_b_in, v_rpb, v_w_att_o, v_conv_w, v_conv_b, v_w_rg_a,
                         v_b_rg_a, v_w_rg_i, v_b_rg_i, v_lru_lambda, v_w_rec_o, v_w_out, v_ln2_g,
                         v_w_ff1, v_w_ff2, v_lnf_g)))
    xi, yi, ci = _position()

    shard = {t: w[n][0].T.astype(BF16) for n, t in TRANSPOSED.items()}
    shard.update({n: w[n][0].astype(BF16) for n in ROW_SHARDED})
    shard["chan"] = _chan_bits([w[n][0] for n, _ in CHAN])
    first, later = ("w_in_t", "chan"), ("w_rec_o", "w_out", "w_att_o_t", "w_ff1_t", "w_ff2")
    *gathered, done = _all_gather([shard[n] for n in first], "weight_all_gather")
    p = dict(zip(first, gathered))
    send_sems, recv_sems, sent, zones, token = _gather_start([shard[n] for n in later], done,
                                                             "weight_gather_start")

    travelling = {"shards": sent, "zones": zones}
    stages = (("w_rec_o", "w_out", "w_att_o_t"), ("w_ff1_t", "w_ff2"))

    passing = {}

    def late_weights(after, stage):
        which = [later.index(n) for n in stages[stage]]
        travelling["shards"], travelling["zones"] = _gather_wait(
            send_sems, recv_sems, travelling["shards"], travelling["zones"], which, after,
            "weight_gather_wait_%d" % stage)
        passing[stage] = _pass_on_start(
            [shard[n].shape[0] for n in stages[stage]], [travelling["zones"][i] for i in which],
            PASS_ON_IDS[stage], "weight_pass_on_start_%d" % stage)
        return passing[stage][-1]

    def late_weights_ready(after, stage):
        pass_send_sems, pass_recv_sems, pass_zones, _ = passing[stage]
        return dict(zip(stages[stage], _pass_on_wait(
            [shard[n].shape[0] for n in stages[stage]], pass_send_sems, pass_recv_sems, pass_zones, after,
            "weight_pass_on_wait_%d" % stage)))

    chan = _chan_from_bits(p.pop("chan"))
    r0 = 0
    for n, rows in CHAN:
        p[n] = chan[r0:r0 + rows]
        r0 += rows
    p.update(ln1_g=w["ln1_g"], b_in=w["b_in"] + token[0, 0], rpb=w["rpb"][0], conv_b=w["conv_b"],
             w_rg_a=w["w_rg_a"][0], w_rg_i=w["w_rg_i"][0], ln2_g=w["ln2_g"],
             lnf_g=w["lnf_g"].reshape(1, D))

    core = jnp.reshape(ci, (1,)).astype(jnp.int32)
    chip = jnp.reshape(2 * xi + yi, (1,)).astype(jnp.int32)
    first_sections = tuple(s for s in SECTIONS if s[0] in ("w_ff1_t", "w_ff2"))
    late_sections = SECTIONS[:1]
    early_sections = tuple(s for s in SECTIONS[1:] if s not in first_sections)
    in_flight = {}

    def pair_sum_and_send(group, sections, after):
        send_sems, recv_sems, sect, zones, _ = in_flight["pair_" + group]
        sect, got = _pair_exchange_wait(sections, send_sems, recv_sems, sect, zones, after,
                                        "grad_pair_exchange_wait_" + group)
        parts = _pair_add(sections, sect, got, core, "grad_pair_add_" + group)
        in_flight[group] = _chip_exchange_start(sections, parts, "grad_chip_exchange_start_" + group)
        return in_flight[group][-1]

    def pair_exchange_at_once(group, sections, grads, barrier_id):
        in_flight["pair_" + group] = _pair_exchange_start(
            sections, [grads[n] for n, _, _ in sections], barrier_id, "grad_pair_exchange_start_" + group)
        return pair_sum_and_send(group, sections, in_flight["pair_" + group][-1])

    def reduce_first(grads, after):
        if grads is None:
            return pair_sum_and_send("first", first_sections, after)
        in_flight["pair_first"] = _pair_exchange_start(
            first_sections, [grads[n] for n, _, _ in first_sections], PAIR_FIRST_ID,
            "grad_pair_exchange_start_first")
        return in_flight["pair_first"][-1]

    def reduce_early(grads):
        chan_g = jnp.concatenate([grads[n] for n, _ in CHAN], axis=0)
        chan_g = chan_g.reshape(N_CHAN_ROWS, N_DEV, LANES).transpose(1, 0, 2).astype(BF16)
        chan_g = jnp.pad(chan_g.reshape(N_DEV, -1), ((0, 0), (0, CHAN_BLOCK_ROWS * D - N_CHAN_ROWS * LANES)))
        grads["chan"] = chan_g.reshape(N_DEV * CHAN_BLOCK_ROWS, D)
        grads["gates"] = jnp.concatenate([grads[n].reshape(-1, D) for n in GATE_BLOCKS], axis=0).astype(BF16)
        return pair_exchange_at_once("early", early_sections, grads, PAIR_EARLY_ID)[0, 0]

    def finish(group, sections, after, name):
        send_sems, recv_sems, parts, zones, _ = in_flight[group]
        parts, far = _chip_exchange_wait(sections, send_sems, recv_sems, parts, zones, after,
                                         "grad_chip_exchange_wait_" + name)
        return dict(zip((n for n, _, _ in sections),
                        _grad_finish(sections, parts, far, chip, "grad_finish_" + name)))

    summed = {}

    def reduce_late(grads):
        in_flight["pair_late"] = _pair_exchange_start(
            late_sections, [grads[n] for n, _, _ in late_sections], PAIR_LATE_ID,
            "grad_pair_exchange_start_late")
        summed.update(finish("first", first_sections, in_flight["pair_late"][-1], "first"))
        summed.update(finish("early", early_sections, summed["w_ff2"], "early"))
        return pair_sum_and_send("late", late_sections, summed["gates"])

    loss_part, grad_x, grads = _local_step(x[0], loss_target[0], p, late_weights, late_weights_ready,
                                           reduce_first, reduce_early, reduce_late)

    flat = jnp.concatenate([grads[n].reshape(-1) for n, _ in REPLICATED if n not in GATE_BLOCKS]
                           + [loss_part.reshape(-1)])
    n_small = flat.shape[0]
    flat = jnp.pad(flat, (0, SMALL_ROWS * LANES - n_small)).reshape(SMALL_ROWS, LANES)
    *small_gather, small_started = _gather_start([flat, summed["gates"]], None, "small_grad_gather_start")

    g, delta, new_m, new_v = {}, {}, {}, {}

    def update(n, g2, shape2, after=None):
        d2, m2, v2 = _adamw(w[n].reshape(shape2), g2, m[n].reshape(shape2), v[n].reshape(shape2),
                            "adamw_" + n, after)
        g[n], delta[n], new_m[n], new_v[n] = (a.reshape(w[n].shape) for a in (g2, d2, m2, v2))

    for n in ROW_SHARDED:
        update(n, summed[n], summed[n].shape, small_started)
    for n, t in TRANSPOSED.items():
        if t in summed:
            update(n, summed[t].T, summed[t].shape[::-1], small_started)

    small_sent, small_zones = _gather_wait(
        *small_gather, range(2), _after_all(list(delta.values()), "sharded_updates_done"),
        "small_grad_gather_wait")
    small_parts, gate_sum = _gather_pass_on([a.shape[0] for a in small_sent], small_zones, SMALL_PASS_ON_ID,
                                            "small_grad_gather_pass_on")
    small = _sum_devices(small_parts, SMALL_ROWS, "small_grad_sum").reshape(-1)
    loss = small[n_small - 1]

    small_params = []
    o = 0
    for n, shape2 in REPLICATED:
        if n in GATE_BLOCKS:
            k, rows = GATE_BLOCKS.index(n), gate_sum.shape[0] // len(GATE_BLOCKS)
            update(n, gate_sum[k * rows:(k + 1) * rows].reshape(shape2), shape2)
        else:
            size = shape2[0] * shape2[1]
            small_params.append((n, small[o:o + size].reshape(shape2), shape2))
            o += size
    chan_back = summed["chan"].reshape(-1)[:N_CHAN_ROWS * LANES].reshape(N_CHAN_ROWS, LANES)
    r0 = 0
    for n, rows in CHAN:
        small_params.append((n, chan_back[r0:r0 + rows], (rows, LANES)))
        r0 += rows
    results = _adamw_small([(w[n].reshape(s2), g2, m[n].reshape(s2), v[n].reshape(s2))
                            for n, g2, s2 in small_params], "adamw_vectors")
    for (n, g2, _), (d2, m2, v2) in zip(small_params, results):
        g[n], delta[n], new_m[n], new_v[n] = (a.reshape(w[n].shape) for a in (g2, d2, m2, v2))

    summed = finish("late", late_sections, _after_all(list(delta.values()), "updates_done"), "late")
    g_t = summed["w_in_t"]
    results = _adamw(w["w_in"][0].T, g_t, m["w_in"][0].T, v["w_in"][0].T, "adamw_w_in")
    g["w_in"], delta["w_in"], new_m["w_in"], new_v["w_in"] = (a.T[None] for a in (g_t, *results))

    return (loss, grad_x[None], *[g[n] for n in NAMES], *[delta[n] for n in NAMES],
            *[new_m[n] for n in NAMES], *[new_v[n] for n in NAMES])
```

```python
import math

import numpy as np
import jax
import jax.numpy as jnp
from jax import lax
from jax.experimental import pallas as pl
from jax.experimental.pallas import tpu as pltpu

F32 = jnp.float32
BF16 = jnp.bfloat16

T = 2048
D = 1024
D_ATT = 512
D_REC = 1024
D_FF = 4096
D_IN = 5632
N_HEADS = 8
DH = 64
GRID_W = 64
ROWS = T // GRID_W
WIN_H = 8
WIN_W = 16
KWIN = WIN_H * GRID_W
N_RPB_R = 2 * WIN_H - 1
N_RPB_C = 2 * WIN_W - 1
N_REC_BLOCKS = 16
REC_BLOCK = 64
CG = 128
N_CG = D_REC // CG
LRU_C = 8.0
EPS = 1e-6
N_DEV = 8
N_CHIPS = 4
LANES = 128

ADAM_LR = 0.001
ADAM_B1 = 0.9
ADAM_B2 = 0.999
ADAM_EPS = 1e-08
ADAM_WD = 0.01
ADAM_STEP = 10

MESH_AXES = ("x", "y", "c")
VMEM_LIMIT = 56 * 1024 * 1024

TILE = 512
DZ_ARRAYS = ((0, 3, 1), (3, 4, 2), (7, 4, 2))
N_DZ_TILES = D_IN // TILE


def _params(**kw):
    return pltpu.CompilerParams(vmem_limit_bytes=VMEM_LIMIT, **kw)


HG = 4
HQ = HG * GRID_W
HC = HG * DH


def _att_tables():
    rq = np.arange(GRID_W)
    kc = np.arange(KWIN) % GRID_W
    win_start = np.clip(rq - WIN_W // 2, 0, GRID_W - WIN_W)
    valid = (kc[None, :] >= win_start[:, None]) & (kc[None, :] < win_start[:, None] + WIN_W)
    same_head = (np.arange(HQ)[:, None] // GRID_W) == (np.arange(HC)[None, :] // DH)
    return valid.astype(np.float32), same_head.astype(np.float32)


def _pair_mask():
    half = np.arange(2 * DH) // DH
    return (half[:, None] == half[None, :]).astype(np.float32)


def _dup_table():
    return np.concatenate([np.eye(REC_BLOCK, dtype=np.float32)] * 2, axis=1)


def _sigmoid(x):
    return 0.5 * jnp.tanh(0.5 * x) + 0.5


def _softplus(x):
    return jnp.maximum(x, 0.0) + jnp.log(1.0 + jnp.exp(-jnp.abs(x)))


def _one_minus_square(log_a, a):
    x = 2.0 * log_a
    series = -x * (1.0 + x * (0.5 + x * (1.0 / 6.0)))
    return jnp.where(x > -0.02, series, 1.0 - a * a)


_GELU_C = math.sqrt(2.0 / math.pi)


def _gelu_and_grad(x):
    x2 = x * x
    inner = _GELU_C * (x + 0.044715 * x * x2)
    t = jnp.tanh(inner)
    g = 0.5 * x * (1.0 + t)
    dg = 0.5 * (1.0 + t) + 0.5 * x * (1.0 - t * t) * _GELU_C * (1.0 + 3.0 * 0.044715 * x2)
    return g, dg


def _dot(a, b):
    return jnp.dot(a, b, preferred_element_type=F32)


def _dot_nt(a, b):
    return lax.dot_general(a, b, (((1,), (1,)), ((), ())), preferred_element_type=F32)


def _dot_tn(a, b):
    return lax.dot_general(a, b, (((0,), (0,)), ((), ())), preferred_element_type=F32)


def _dot_exact(a, b):
    return jnp.dot(a, b, precision=lax.Precision.HIGHEST, preferred_element_type=F32)


def _shift_rows(x, s):
    n = x.shape[0]
    rows = lax.broadcasted_iota(jnp.int32, x.shape, 0)
    y = pltpu.roll(x, s % n, 0)
    if s > 0:
        return jnp.where(rows >= s, y, 0.0)
    return jnp.where(rows < n + s, y, 0.0)


def _rms_bwd(dh, xh, r, g):
    dxh = dh * g
    return r * (dxh - xh * jnp.mean(dxh * xh, axis=-1, keepdims=True))


def _matmul(a, b, mode, out_dtype, name, tm=512, tn=1024, tk=2048):
    if mode == "nn":
        (m, k), (k2, n) = a.shape, b.shape
    elif mode == "nt":
        (m, k), (n, k2) = a.shape, b.shape
    else:
        (k, m), (k2, n) = a.shape, b.shape
    assert k == k2
    tm, tn, tk = min(tm, m), min(tn, n), min(tk, k)
    assert m % tm == 0 and n % tn == 0 and k % tk == 0
    nk = k // tk
    dot = {"nn": _dot, "nt": _dot_nt, "tn": _dot_tn}[mode]

    def body(a_ref, b_ref, o_ref, acc):
        kk = pl.program_id(2)
        part = dot(a_ref[...].astype(BF16), b_ref[...].astype(BF16))
        if nk == 1:
            o_ref[...] = part.astype(out_dtype)
            return

        @pl.when(kk == 0)
        def _():
            acc[...] = part

        @pl.when(kk > 0)
        def _():
            acc[...] += part

        @pl.when(kk == nk - 1)
        def _():
            o_ref[...] = acc[...].astype(out_dtype)

    if mode == "tn":
        a_spec = pl.BlockSpec((tk, tm), lambda i, j, kk: (kk, i))
    else:
        a_spec = pl.BlockSpec((tm, tk), lambda i, j, kk: (i, kk))
    if mode == "nt":
        b_spec = pl.BlockSpec((tn, tk), lambda i, j, kk: (j, kk))
    else:
        b_spec = pl.BlockSpec((tk, tn), lambda i, j, kk: (kk, j))
    return pl.pallas_call(
        body, name=name,
        out_shape=jax.ShapeDtypeStruct((m, n), out_dtype),
        grid=(m // tm, n // tn, nk),
        in_specs=[a_spec, b_spec],
        out_specs=pl.BlockSpec((tm, tn), lambda i, j, kk: (i, j)),
        scratch_shapes=[pltpu.VMEM((tm, tn) if nk > 1 else (8, LANES), F32)],
        compiler_params=_params(dimension_semantics=("parallel", "parallel", "arbitrary")),
    )(a, b)


def _in_proj(x, g1, w_in_t, b_in):
    tm = 512

    def body(x_ref, g_ref, w_hbm, b_ref, qkv_ref, uy_ref, gg_ref, h_ref, w):
        @pl.when(pl.program_id(0) == 0)
        def _():
            pltpu.sync_copy(w_hbm, w)

        xv = x_ref[...]
        r = lax.rsqrt(jnp.mean(xv * xv, axis=-1, keepdims=True) + EPS)
        h = ((xv * r) * g_ref[...]).astype(BF16)
        h_ref[...] = h
        row0 = 0
        for ref in (qkv_ref, uy_ref, gg_ref):
            for c0 in range(0, ref.shape[1], TILE):
                z = _dot_nt(h, w[row0:row0 + TILE, :]) + b_ref[:, row0:row0 + TILE]
                ref[:, c0:c0 + TILE] = z.astype(ref.dtype)
                row0 += TILE

    tok = lambda width: pl.BlockSpec((tm, width), lambda i: (i, 0))
    return pl.pallas_call(
        body, name="in_proj",
        out_shape=(jax.ShapeDtypeStruct((T, 3 * D_ATT), BF16),
                   jax.ShapeDtypeStruct((T, 2 * D_REC), F32),
                   jax.ShapeDtypeStruct((T, 2 * D), F32),
                   jax.ShapeDtypeStruct((T, D), BF16)),
        grid=(T // tm,),
        in_specs=[tok(D), pl.BlockSpec((1, D), lambda i: (0, 0)), pl.BlockSpec(memory_space=pl.ANY),
                  pl.BlockSpec((1, D_IN), lambda i: (0, 0))],
        out_specs=(tok(3 * D_ATT), tok(2 * D_REC), tok(2 * D), tok(D)),
        scratch_shapes=[pltpu.VMEM((D_IN, D), BF16)],
        compiler_params=_params(dimension_semantics=("arbitrary",)),
    )(x, g1, w_in_t, b_in)


def _dz_specs(rows, tile_of, row_of):
    def spec(off, n, per_plane):
        def index(*ids):
            t = jnp.clip(tile_of(*ids) - off, 0, n - 1)
            return (t // per_plane, row_of(*ids), t % per_plane)
        return pl.BlockSpec((1, rows, TILE), index)
    return [spec(off, n, per) for off, n, per in DZ_ARRAYS]


def _dh_norm1_bwd(dz, w_in_t, x, g1, dx1, after):
    tm = 512

    def body(dqkv_ref, duy_ref, dgg_ref, w_hbm, x_ref, g_ref, dx1_ref, after_ref, gx_ref, dg_ref, w):
        @pl.when(pl.program_id(0) == 0)
        def _():
            pltpu.sync_copy(w_hbm, w)
            dg_ref[...] = jnp.zeros_like(dg_ref)

        dh, row0 = None, 0
        for ref in (dqkv_ref, duy_ref, dgg_ref):
            for plane in range(ref.shape[0]):
                cols = ref.shape[2]
                part = _dot(ref[plane], w[row0:row0 + cols, :])
                dh = part if dh is None else dh + part
                row0 += cols
        xv = x_ref[...]
        r = lax.rsqrt(jnp.mean(xv * xv, axis=-1, keepdims=True) + EPS)
        xh = xv * r
        dg_ref[...] += jnp.sum(dh * xh, axis=0, keepdims=True)
        gx_ref[...] = dx1_ref[...] + _rms_bwd(dh, xh, r, g_ref[...])

    tok = pl.BlockSpec((tm, D), lambda i: (i, 0))
    vec = pl.BlockSpec((1, D), lambda i: (0, 0))
    planes = lambda a: pl.BlockSpec((a.shape[0], tm, a.shape[2]), lambda i: (0, i, 0))
    return pl.pallas_call(
        body, name="dh_norm1_bwd",
        out_shape=(jax.ShapeDtypeStruct((T, D), F32), jax.ShapeDtypeStruct((1, D), F32)),
        grid=(T // tm,),
        in_specs=[planes(a) for a in dz] + [pl.BlockSpec(memory_space=pl.ANY), tok, vec, tok,
                                            pl.BlockSpec(memory_space=pl.ANY)],
        out_specs=(tok, vec),
        scratch_shapes=[pltpu.VMEM((D_IN, D), BF16)],
        compiler_params=_params(dimension_semantics=("arbitrary",)),
    )(*dz, w_in_t, x, g1, dx1, after)


def _grad_w_in(dz, h):
    def body(*refs):
        seg_refs = refs[:3]
        h_ref, gw_ref, gb_ref = refs[3:]
        j = pl.program_id(0)

        for s, (off, n, _) in enumerate(DZ_ARRAYS):
            @pl.when((j >= off) & (j < off + n))
            def _(s=s):
                a = seg_refs[s][0]
                gw_ref[...] = _dot_tn(a, h_ref[...]).astype(BF16)
                gb_ref[...] = jnp.sum(a.astype(F32), axis=0, keepdims=True)

    return pl.pallas_call(
        body, name="grad_w_in",
        out_shape=(jax.ShapeDtypeStruct((D_IN, D), BF16), jax.ShapeDtypeStruct((1, D_IN), F32)),
        grid=(N_DZ_TILES,),
        in_specs=_dz_specs(T, lambda j: j, lambda j: 0) + [pl.BlockSpec((T, D), lambda j: (0, 0))],
        out_specs=(pl.BlockSpec((TILE, D), lambda j: (j, 0)), pl.BlockSpec((1, TILE), lambda j: (0, j))),
        compiler_params=_params(dimension_semantics=("parallel",)),
    )(*dz, h)


def _rpb_rows(rpb):
    padded = jnp.pad(rpb, ((0, 0), (0, 0), (0, GRID_W - N_RPB_C)))
    rows = [padded[:, WIN_H - 1 - oi: 2 * WIN_H - 1 - oi].reshape(N_HEADS // HG, HG, KWIN)
            for oi in range(WIN_H)]
    return jnp.stack(rows, axis=0)


SKEW = KWIN - (WIN_W - 1)


MASKED = -1e30


def _bias_tiles(rows_ref, valid, bias_s):
    for oi in range(WIN_H):
        for hh in range(HG):
            row = jnp.broadcast_to(rows_ref[oi, 0, hh:hh + 1, :], (GRID_W, KWIN))
            tile = pltpu.roll(row, SKEW, 1, stride=1, stride_axis=0)
            bias_s[oi, hh * GRID_W:(hh + 1) * GRID_W, :] = jnp.where(valid, tile, MASKED)


def _bias_tile_grads(gb_s, flip, out_ref):
    for oi in range(WIN_H):
        for hh in range(HG):
            g = _dot_exact(flip, gb_s[oi, hh * GRID_W:(hh + 1) * GRID_W, :])
            back = pltpu.roll(g, KWIN - (GRID_W - WIN_W), 1, stride=1, stride_axis=0)
            out_ref[0, oi, hh:hh + 1, :] = jnp.sum(back, axis=0, keepdims=True)


def _rpb_fold(row_grads):
    g = row_grads.transpose(1, 0, 2, 3).reshape(WIN_H, N_HEADS, WIN_H, GRID_W)
    g = g.transpose(0, 2, 1, 3)

    def body(g_ref, o_ref):
        for dr in range(N_RPB_R):
            terms = [g_ref[oi, i] for oi in range(WIN_H) for i in range(WIN_H) if i - oi + WIN_H - 1 == dr]
            acc = terms[0]
            for term in terms[1:]:
                acc = acc + term
            o_ref[dr] = acc

    out = pl.pallas_call(
        body, name="rpb_fold",
        out_shape=jax.ShapeDtypeStruct((N_RPB_R, N_HEADS, GRID_W), F32),
    )(g)
    return out.transpose(1, 0, 2)[:, :, :N_RPB_C]


ATT_GROUPS = N_HEADS // HG
ATT_UNROLL = 8


def _stacked(rows64, same_head):
    return jnp.where(same_head, jnp.concatenate([rows64] * HG, axis=0), jnp.zeros((), BF16))


def _own_heads(stacked):
    head = lax.broadcasted_iota(jnp.int32, (GRID_W, HC), 1) // DH
    out = stacked[:GRID_W]
    for h in range(1, HG):
        out = jnp.where(head == h, stacked[h * GRID_W:(h + 1) * GRID_W], out)
    return out


def _att_scores(q_ref, k_ref, bias_ref, same_head, r):
    rs = jnp.clip(r - WIN_H // 2, 0, ROWS - WIN_H)
    oi = r - rs
    q0 = pl.multiple_of(r * GRID_W, GRID_W)
    k0 = pl.multiple_of(rs * GRID_W, GRID_W)
    q2 = _stacked(q_ref[pl.ds(q0, GRID_W), :] * (DH ** -0.5), same_head)
    kw = k_ref[pl.ds(k0, KWIN), :]
    s = _dot_nt(q2, kw) + bias_ref[oi]
    e = jnp.exp(s - jnp.max(s, axis=-1, keepdims=True))
    return e, 1.0 / jnp.sum(e, axis=-1, keepdims=True), q2, kw, q0, k0, oi


def _att_specs():
    col = lambda off: pl.BlockSpec((T, HC), lambda g: (0, g + off * ATT_GROUPS))
    tables = [pl.BlockSpec((WIN_H, 1, HG, KWIN), lambda g: (0, g, 0, 0)),
              pl.BlockSpec((GRID_W, KWIN), lambda g: (0, 0)),
              pl.BlockSpec((HQ, HC), lambda g: (0, 0))]
    return col, tables, pltpu.VMEM((WIN_H, HQ, KWIN), F32)


def _att_fwd(qkv, bias_rows):
    valid_np, same_head_np = _att_tables()

    def body(q_ref, k_ref, v_ref, rows_ref, valid_ref, head_ref, o_ref, bias_s):
        same_head = head_ref[...] > 0.5
        _bias_tiles(rows_ref, valid_ref[...] > 0.5, bias_s)

        def row(r, carry):
            e, rl, _, _, q0, k0, _ = _att_scores(q_ref, k_ref, bias_s, same_head, r)
            o2 = _dot((e * rl).astype(BF16), v_ref[pl.ds(k0, KWIN), :])
            o_ref[pl.ds(q0, GRID_W), :] = _own_heads(o2).astype(BF16)
            return carry

        lax.fori_loop(0, ROWS, row, 0, unroll=ATT_UNROLL)

    col, tables, tiles = _att_specs()
    return pl.pallas_call(
        body, name="att_fwd",
        out_shape=jax.ShapeDtypeStruct((T, D_ATT), BF16),
        grid=(ATT_GROUPS,),
        in_specs=[col(0), col(1), col(2)] + tables,
        out_specs=col(0),
        scratch_shapes=[tiles],
        compiler_params=_params(dimension_semantics=("parallel",)),
    )(qkv, qkv, qkv, bias_rows, jnp.asarray(valid_np), jnp.asarray(same_head_np))


def _att_bwd(qkv, bias_rows, datt, after):
    valid_np, same_head_np = _att_tables()

    def body(q_ref, k_ref, v_ref, do_ref, rows_ref, valid_ref, head_ref, flip_ref,
             dqkv_ref, grows_ref, dk_acc, dv_acc, bias_s, gb_s):
        same_head = head_ref[...] > 0.5
        dk_acc[...] = jnp.zeros_like(dk_acc)
        dv_acc[...] = jnp.zeros_like(dv_acc)
        gb_s[...] = jnp.zeros_like(gb_s)
        _bias_tiles(rows_ref, valid_ref[...] > 0.5, bias_s)

        def row(r, carry):
            e, rl, q2, kw, q0, k0, oi = _att_scores(q_ref, k_ref, bias_s, same_head, r)
            do2 = _stacked(do_ref[pl.ds(q0, GRID_W), :], same_head)
            vw = v_ref[pl.ds(k0, KWIN), :]
            p = e * rl
            dp = _dot_nt(do2, vw)
            ds = p * (dp - jnp.sum(dp * p, axis=-1, keepdims=True))
            p16 = p.astype(BF16)
            ds16 = ds.astype(BF16)
            dv_acc[pl.ds(k0, KWIN), :] += _dot_tn(p16, do2)
            dk_acc[pl.ds(k0, KWIN), :] += _dot_tn(ds16, q2)
            dq2 = _dot(ds16, kw) * (DH ** -0.5)
            dqkv_ref[0, pl.ds(q0, GRID_W), :] = _own_heads(dq2).astype(BF16)
            gb_s[oi] += ds
            return carry

        lax.fori_loop(0, ROWS, row, 0, unroll=ATT_UNROLL)
        dqkv_ref[1] = dk_acc[...].astype(BF16)
        dqkv_ref[2] = dv_acc[...].astype(BF16)
        _bias_tile_grads(gb_s, flip_ref[...], grows_ref)

    col, tables, tiles = _att_specs()
    return pl.pallas_call(
        body, name="att_bwd",
        out_shape=(jax.ShapeDtypeStruct((3, T, D_ATT), BF16),
                   jax.ShapeDtypeStruct((ATT_GROUPS, WIN_H, HG, KWIN), F32)),
        grid=(ATT_GROUPS,),
        in_specs=[col(0), col(1), col(2), col(0)] + tables + [pl.BlockSpec((GRID_W, GRID_W), lambda g: (0, 0))],
        out_specs=(pl.BlockSpec((3, T, HC), lambda g: (0, 0, g)),
                   pl.BlockSpec((1, WIN_H, HG, KWIN), lambda g: (g, 0, 0, 0))),
        scratch_shapes=[pltpu.VMEM((T, HC), F32), pltpu.VMEM((T, HC), F32), tiles, tiles],
        compiler_params=_params(dimension_semantics=("parallel",)),
    )(qkv, qkv, qkv, datt, bias_rows, jnp.asarray(valid_np) + after, jnp.asarray(same_head_np),
      jnp.asarray(np.eye(GRID_W, dtype=np.float32)[::-1].copy()))


def _conv_taps(up):
    return (_shift_rows(up, 2), _shift_rows(up, 1), up, _shift_rows(up, -1))


def _pair_block_diag(w_pair, dup, same_half):
    return jnp.where(same_half, _dot(w_pair.astype(BF16), dup), 0.0).astype(BF16)


def _gates(u, u16, wa, ba, wi, bi, lam):
    r = _sigmoid(_dot(u16, wa) + ba)
    ig = _sigmoid(_dot(u16, wi) + bi)
    sp = _softplus(-lam)
    log_a = (-LRU_C) * r * sp
    a = jnp.exp(log_a)
    mult2 = jnp.maximum(_one_minus_square(log_a, a), 0.0)
    return r, ig, sp, a, jnp.sqrt(mult2), mult2


SCAN_BLOCKS = 8


def _scans(jobs):
    c = jobs[0][0].shape[1]
    nblk = T // 8
    rows = lax.broadcasted_iota(jnp.int32, (8, c), 0)

    def block(a, b, reverse):
        for s in (1, 2, 4):
            if reverse:
                keep = rows < 8 - s
                a_s = jnp.where(keep, pltpu.roll(a, 8 - s, 0), 1.0)
                b_s = jnp.where(keep, pltpu.roll(b, 8 - s, 0), 0.0)
            else:
                keep = rows >= s
                a_s = jnp.where(keep, pltpu.roll(a, s, 0), 1.0)
                b_s = jnp.where(keep, pltpu.roll(b, s, 0), 0.0)
            b = a * b_s + b
            a = a * a_s
        return a, b

    def step(i, carry):
        out = []
        for (a_ref, b_ref, h_ref, reverse), h_prev in zip(jobs, carry):
            for u in range(SCAN_BLOCKS):
                blk = i * SCAN_BLOCKS + u
                if reverse:
                    blk = nblk - 1 - blk
                t0 = pl.multiple_of(blk * 8, 8)
                a, b = block(a_ref[pl.ds(t0, 8), :], b_ref[pl.ds(t0, 8), :], reverse)
                h = a * h_prev + b
                h_ref[pl.ds(t0, 8), :] = h
                h_prev = jnp.broadcast_to(h[0:1] if reverse else h[7:8], (8, c))
            out.append(h_prev)
        return tuple(out)

    lax.fori_loop(0, nblk // SCAN_BLOCKS, step, tuple(jnp.zeros((8, c), F32) for _ in jobs))


def _rec_specs():
    tok = lambda off: pl.BlockSpec((T, CG), lambda g: (0, g + off))
    per_ch = lambda rows: pl.BlockSpec((rows, CG), lambda g: (0, g))
    wspec = pl.BlockSpec((2, 1, CG, REC_BLOCK), lambda g: (0, g, 0, 0))
    const = lambda shape: pl.BlockSpec(shape, lambda g: (0, 0))
    return tok, per_ch, wspec, const


def _rec_fwd(uy, conv_w, conv_b, w_a, b_a, w_i, b_i, lam):
    tok, per_ch, wspec, const = _rec_specs()

    def body(up_ref, yb_ref, cw_ref, cb_ref, wa_ref, ba_ref, wi_ref, bi_ref, lam_ref, dup_ref, half_ref,
             hf_ref, hb_ref, yrec_ref, am_ref, bx_f, bx_b):
        dup = dup_ref[...]
        same_half = half_ref[...] > 0.5
        taps = _conv_taps(up_ref[...])
        u = cb_ref[...]
        for j in range(4):
            u = u + taps[j] * cw_ref[j:j + 1, :]
        u16 = u.astype(BF16)
        for d, bx_s in enumerate((bx_f, bx_b)):
            wa = _pair_block_diag(wa_ref[d, 0], dup, same_half)
            wi = _pair_block_diag(wi_ref[d, 0], dup, same_half)
            _, ig, _, a, mult, _ = _gates(u, u16, wa, ba_ref[d:d + 1, :], wi, bi_ref[d:d + 1, :],
                                       lam_ref[d:d + 1, :])
            am_ref[2 * d] = a
            am_ref[2 * d + 1] = mult
            bx_s[...] = mult * (ig * u)
        _scans([(am_ref.at[0], bx_f, hf_ref, False), (am_ref.at[2], bx_b, hb_ref, True)])
        gelu, _ = _gelu_and_grad(yb_ref[...])
        yrec_ref[...] = ((hf_ref[...] + hb_ref[...]) * gelu).astype(BF16)

    return pl.pallas_call(
        body, name="rec_fwd",
        out_shape=(jax.ShapeDtypeStruct((T, D_REC), F32), jax.ShapeDtypeStruct((T, D_REC), F32),
                   jax.ShapeDtypeStruct((T, D_REC), BF16), jax.ShapeDtypeStruct((4, T, D_REC), F32)),
        grid=(N_CG,),
        in_specs=[tok(0), tok(N_CG), per_ch(4), per_ch(1), wspec, per_ch(2), wspec, per_ch(2), per_ch(2),
                  const((REC_BLOCK, CG)), const((CG, CG))],
        out_specs=(tok(0), tok(0), tok(0), pl.BlockSpec((4, T, CG), lambda g: (0, 0, g))),
        scratch_shapes=[pltpu.VMEM((T, CG), F32)] * 2,
        compiler_params=_params(dimension_semantics=("parallel",)),
    )(uy, uy, conv_w, conv_b, w_a, b_a, w_i, b_i, lam,
      jnp.asarray(_dup_table(), BF16), jnp.asarray(_pair_mask()))


def _rec_bwd(uy, hf, hb, am, dyrec, conv_w, conv_b, w_a, b_a, w_i, b_i, lam):
    tok, per_ch, wspec, const = _rec_specs()

    def body(up_ref, yb_ref, hf_ref, hb_ref, am_ref, dy_ref, cw_ref, cb_ref, wa_ref, ba_ref, wi_ref, bi_ref,
             lam_ref, dup_ref, dupt_ref, half_ref,
             duy_ref, dcw_ref, dcb_ref, dwa_ref, dba_ref, dwi_ref, dbi_ref, dlam_ref,
             a_s0, a_s1, dh_s, g_s0, g_s1):
        dup = dup_ref[...]
        dup_t = dupt_ref[...]
        same_half = half_ref[...] > 0.5
        taps = _conv_taps(up_ref[...])
        u = cb_ref[...]
        for j in range(4):
            u = u + taps[j] * cw_ref[j:j + 1, :]
        u16 = u.astype(BF16)
        gelu, dgelu = _gelu_and_grad(yb_ref[...])
        dy = dy_ref[...]
        duy_ref[1] = (dy * (hf_ref[...] + hb_ref[...]) * dgelu).astype(BF16)
        dh_s[...] = dy * gelu
        a_s0[...] = _shift_rows(am_ref[0], -1)
        a_s1[...] = _shift_rows(am_ref[2], 1)
        _scans([(a_s0, dh_s, g_s0, True), (a_s1, dh_s, g_s1, False)])
        du = jnp.zeros((T, CG), F32)
        for d, g_s in enumerate((g_s0, g_s1)):
            reverse = d == 1
            wa = _pair_block_diag(wa_ref[d, 0], dup, same_half)
            wi = _pair_block_diag(wi_ref[d, 0], dup, same_half)
            lam_d = lam_ref[d:d + 1, :]
            r = _sigmoid(_dot(u16, wa) + ba_ref[d:d + 1, :])
            ig = _sigmoid(_dot(u16, wi) + bi_ref[d:d + 1, :])
            sp = _softplus(-lam_d)
            a, mult = am_ref[2 * d], am_ref[2 * d + 1]
            mult2 = mult * mult
            g = g_s[...]
            h_prev = _shift_rows(hb_ref[...], -1) if reverse else _shift_rows(hf_ref[...], 1)
            da = g * h_prev
            dmult = g * (ig * u)
            dig = g * mult * u
            du = du + g * mult * ig
            dmult_dlog = jnp.where(mult2 > 0.0, -(a * a) * lax.rsqrt(mult2), 0.0)
            dlog_a = da * a + dmult * dmult_dlog
            dr = dlog_a * ((-LRU_C) * sp)
            dsp = jnp.sum(dlog_a * ((-LRU_C) * r), axis=0, keepdims=True)
            dlam_ref[d:d + 1, :] = dsp * (-_sigmoid(-lam_d))
            dga = dr * r * (1.0 - r)
            dgi = dig * ig * (1.0 - ig)
            dga16 = dga.astype(BF16)
            dgi16 = dgi.astype(BF16)
            du = du + _dot_nt(dga16, wa) + _dot_nt(dgi16, wi)
            dwa_ref[d, 0] = _dot_exact(jnp.where(same_half, _dot_tn(u16, dga16), 0.0), dup_t)
            dwi_ref[d, 0] = _dot_exact(jnp.where(same_half, _dot_tn(u16, dgi16), 0.0), dup_t)
            dba_ref[d:d + 1, :] = jnp.sum(dga, axis=0, keepdims=True)
            dbi_ref[d:d + 1, :] = jnp.sum(dgi, axis=0, keepdims=True)
        dcb_ref[...] = jnp.sum(du, axis=0, keepdims=True)
        for j in range(4):
            dcw_ref[j:j + 1, :] = jnp.sum(du * taps[j], axis=0, keepdims=True)
        dup_in = (_shift_rows(du, -2) * cw_ref[0:1, :] + _shift_rows(du, -1) * cw_ref[1:2, :]
                  + du * cw_ref[2:3, :] + _shift_rows(du, 1) * cw_ref[3:4, :])
        duy_ref[0] = dup_in.astype(BF16)

    wshape = jax.ShapeDtypeStruct((2, N_CG, CG, REC_BLOCK), F32)
    vec = lambda rows: jax.ShapeDtypeStruct((rows, D_REC), F32)
    dup_np = _dup_table()
    return pl.pallas_call(
        body, name="rec_bwd",
        out_shape=(jax.ShapeDtypeStruct((2, T, D_REC), BF16),
                   vec(4), vec(1), wshape, vec(2), wshape, vec(2), vec(2)),
        grid=(N_CG,),
        in_specs=[tok(0), tok(N_CG), tok(0), tok(0), pl.BlockSpec((4, T, CG), lambda g: (0, 0, g)), tok(0),
                  per_ch(4), per_ch(1), wspec, per_ch(2), wspec, per_ch(2), per_ch(2),
                  const((REC_BLOCK, CG)), const((CG, REC_BLOCK)), const((CG, CG))],
        out_specs=(pl.BlockSpec((2, T, CG), lambda g: (0, 0, g)),
                   per_ch(4), per_ch(1), wspec, per_ch(2), wspec, per_ch(2), per_ch(2)),
        scratch_shapes=[pltpu.VMEM((T, CG), F32)] * 5,
        compiler_params=_params(dimension_semantics=("parallel",)),
    )(uy, uy, hf, hb, am, dyrec, conv_w, conv_b, w_a, b_a, w_i, b_i, lam,
      jnp.asarray(dup_np, BF16), jnp.asarray(dup_np.T.copy()), jnp.asarray(_pair_mask()))


TM_MIX = 256


def _mix_specs():
    tok = lambda width, blk=0: pl.BlockSpec((TM_MIX, width), lambda i: (i, blk))
    full = lambda shape: pl.BlockSpec(shape, lambda i: (0, 0))
    return tok, full


def _mix_fwd(x, att, yrec, gg, w_att_o_t, w_rec_o, w_out):
    tok, full = _mix_specs()

    def body(x_ref, att_ref, yr_ref, ga_ref, gr_ref, wao_ref, wro_ref, wo_ref, x1_ref, mixed_ref):
        y_att = _dot_nt(att_ref[...], wao_ref[...])
        y_rec = _dot(yr_ref[...], wro_ref[...])
        mixed = (_sigmoid(ga_ref[...]) * y_att + _sigmoid(gr_ref[...]) * y_rec).astype(BF16)
        mixed_ref[...] = mixed
        x1_ref[...] = x_ref[...] + _dot(mixed, wo_ref[...])

    return pl.pallas_call(
        body, name="mix_fwd",
        out_shape=(jax.ShapeDtypeStruct((T, D), F32), jax.ShapeDtypeStruct((T, D), BF16)),
        grid=(T // TM_MIX,),
        in_specs=[tok(D), tok(D_ATT), tok(D_REC), tok(D, 0), tok(D, 1),
                  full((D, D_ATT)), full((D_REC, D)), full((D, D))],
        out_specs=(tok(D), tok(D)),
        compiler_params=_params(dimension_semantics=("parallel",)),
    )(x, att, yrec, gg, gg, w_att_o_t, w_rec_o, w_out)


def _mix_bwd(dx1, att, yrec, gg, w_att_o_t, w_rec_o, w_out, after):
    tok, full = _mix_specs()

    def body(dx_ref, att_ref, yr_ref, ga_ref, gr_ref, wao_ref, wro_ref, wo_ref, after_ref,
             dgg_ref, dya_ref, dyr_ref, datt_ref, dyrp_ref):
        dmixed = _dot_nt(dx_ref[...].astype(BF16), wo_ref[...])
        y_att = _dot_nt(att_ref[...], wao_ref[...])
        y_rec = _dot(yr_ref[...], wro_ref[...])
        sa = _sigmoid(ga_ref[...])
        sr = _sigmoid(gr_ref[...])
        dgg_ref[0] = (dmixed * y_att * sa * (1.0 - sa)).astype(BF16)
        dgg_ref[1] = (dmixed * y_rec * sr * (1.0 - sr)).astype(BF16)
        dya = (dmixed * sa).astype(BF16)
        dyr = (dmixed * sr).astype(BF16)
        dya_ref[...] = dya
        dyr_ref[...] = dyr
        datt_ref[...] = _dot(dya, wao_ref[...]).astype(BF16)
        dyrp_ref[...] = _dot_nt(dyr, wro_ref[...])

    return pl.pallas_call(
        body, name="mix_bwd",
        out_shape=(jax.ShapeDtypeStruct((2, T, D), BF16),
                   jax.ShapeDtypeStruct((T, D), BF16), jax.ShapeDtypeStruct((T, D), BF16),
                   jax.ShapeDtypeStruct((T, D_ATT), BF16), jax.ShapeDtypeStruct((T, D_REC), F32)),
        grid=(T // TM_MIX,),
        in_specs=[tok(D), tok(D_ATT), tok(D_REC), tok(D, 0), tok(D, 1),
                  full((D, D_ATT)), full((D_REC, D)), full((D, D)), pl.BlockSpec(memory_space=pl.ANY)],
        out_specs=(pl.BlockSpec((2, TM_MIX, D), lambda i: (0, i, 0)),
                   tok(D), tok(D), tok(D_ATT), tok(D_REC)),
        compiler_params=_params(dimension_semantics=("parallel",)),
    )(dx1, att, yrec, gg, gg, w_att_o_t, w_rec_o, w_out, after)


TM_FFN = 256
FF_CHUNK = 1024


def _ffn_loss(x1, target, g2, gf, w_ff1_t, w_ff2):
    n_chunks = D_FF // FF_CHUNK

    def body(x1_ref, tg_ref, g2_ref, gf_ref, w1_hbm, w2_hbm,
             loss_ref, dx1_ref, h2_ref, act_ref, dpre_ref, dx2_ref, dg2_ref, dgf_ref,
             w1, w2, relu_s):
        i = pl.program_id(0)

        @pl.when(i == 0)
        def _():
            pltpu.sync_copy(w1_hbm, w1)
            pltpu.sync_copy(w2_hbm, w2)
            loss_ref[...] = jnp.zeros_like(loss_ref)
            dg2_ref[...] = jnp.zeros_like(dg2_ref)
            dgf_ref[...] = jnp.zeros_like(dgf_ref)

        x1v = x1_ref[...]
        r2 = lax.rsqrt(jnp.mean(x1v * x1v, axis=-1, keepdims=True) + EPS)
        xh2 = x1v * r2
        h2 = (xh2 * g2_ref[...]).astype(BF16)
        h2_ref[...] = h2
        x2 = x1v
        for c in range(n_chunks):
            ff = slice(c * FF_CHUNK, (c + 1) * FF_CHUNK)
            rl = jnp.maximum(_dot_nt(h2, w1[ff, :]), 0.0)
            relu_s[:, ff] = rl
            act = (rl * rl).astype(BF16)
            act_ref[:, ff] = act
            x2 = x2 + _dot(act, w2[ff, :])
        r3 = lax.rsqrt(jnp.mean(x2 * x2, axis=-1, keepdims=True) + EPS)
        xh3 = x2 * r3
        err = xh3 * gf_ref[...] - tg_ref[...]
        loss_ref[...] += 0.5 * jnp.sum(jnp.mean(err * err, axis=-1, keepdims=True))
        dy = err * (1.0 / D)
        dgf_ref[...] += jnp.sum(dy * xh3, axis=0, keepdims=True)
        dx2 = _rms_bwd(dy, xh3, r3, gf_ref[...])
        dx2_16 = dx2.astype(BF16)
        dx2_ref[...] = dx2_16
        dh2 = jnp.zeros((TM_FFN, D), F32)
        for c in range(n_chunks):
            ff = slice(c * FF_CHUNK, (c + 1) * FF_CHUNK)
            dpre = (_dot_nt(dx2_16, w2[ff, :]) * (2.0 * relu_s[:, ff])).astype(BF16)
            dpre_ref[:, ff] = dpre
            dh2 = dh2 + _dot(dpre, w1[ff, :])
        dg2_ref[...] += jnp.sum(dh2 * xh2, axis=0, keepdims=True)
        dx1_ref[...] = dx2 + _rms_bwd(dh2, xh2, r2, g2_ref[...])

    tok = lambda width: pl.BlockSpec((TM_FFN, width), lambda i: (i, 0))
    vec = pl.BlockSpec((1, D), lambda i: (0, 0))
    hbm = pl.BlockSpec(memory_space=pl.ANY)
    return pl.pallas_call(
        body, name="ffn_loss",
        out_shape=(jax.ShapeDtypeStruct((8, 128), F32), jax.ShapeDtypeStruct((T, D), F32),
                   jax.ShapeDtypeStruct((T, D), BF16), jax.ShapeDtypeStruct((T, D_FF), BF16),
                   jax.ShapeDtypeStruct((T, D_FF), BF16), jax.ShapeDtypeStruct((T, D), BF16),
                   jax.ShapeDtypeStruct((1, D), F32), jax.ShapeDtypeStruct((1, D), F32)),
        grid=(T // TM_FFN,),
        in_specs=[tok(D), tok(D), vec, vec, hbm, hbm],
        out_specs=(pl.BlockSpec((8, 128), lambda i: (0, 0)), tok(D), tok(D), tok(D_FF), tok(D_FF), tok(D),
                   vec, vec),
        scratch_shapes=[pltpu.VMEM((D_FF, D), BF16), pltpu.VMEM((D_FF, D), BF16),
                        pltpu.VMEM((TM_FFN, D_FF), F32)],
        compiler_params=_params(dimension_semantics=("arbitrary",)),
    )(x1, target, g2, gf, w_ff1_t, w_ff2)


def _local_step(x, target, p, late_weights, late_weights_ready, reduce_first, reduce_early, reduce_late):
    bias = _rpb_rows(p["rpb"])
    pairs = lambda w: w.reshape(2, N_CG, CG, REC_BLOCK)
    w_a, w_i = pairs(p["w_rg_a"]), pairs(p["w_rg_i"])
    rec_params = (p["conv_w"], p["conv_b"], w_a, p["b_rg_a"], w_i, p["b_rg_i"], p["lru_lambda"])

    qkv, uy, gg, h = _in_proj(x, p["ln1_g"], p["w_in_t"], p["b_in"])
    hf, hb, yrec, am = _rec_fwd(uy, *rec_params)
    bias = bias + late_weights(yrec, 0)[0, 0]
    att = _att_fwd(qkv, bias)
    p = {**p, **late_weights_ready(late_weights(att, 1), 0)}
    x1, mixed = _mix_fwd(x, att, yrec, gg, p["w_att_o_t"], p["w_rec_o"], p["w_out"])
    p = {**p, **late_weights_ready(x1, 1)}
    loss8, dx1, h2, act, dpre, dx2, g_ln2, g_lnf = _ffn_loss(
        x1, target, p["ln2_g"], p["lnf_g"], p["w_ff1_t"], p["w_ff2"])

    grads = {"ln2_g": g_ln2, "lnf_g": g_lnf,
             "w_ff1_t": _matmul(dpre, h2, "tn", BF16, "g_w_ff1"),
             "w_ff2": _matmul(act, dx2, "tn", BF16, "g_w_ff2")}
    dgg, dya, dyr, datt, dyrp = _mix_bwd(dx1, att, yrec, gg, p["w_att_o_t"], p["w_rec_o"], p["w_out"],
                                         reduce_first(grads, None))
    lam_after = rec_params[-1] + reduce_first(None, dgg)[0, 0]
    duy, g_cw, g_cb, g_wa, g_ba, g_wi, g_bi, g_lam = _rec_bwd(uy, hf, hb, am, dyrp, *rec_params[:-1], lam_after)
    blocks = lambda g: g.reshape(2, N_REC_BLOCKS, REC_BLOCK, REC_BLOCK)
    grads.update({
        "w_att_o_t": _matmul(dya, att, "tn", BF16, "g_w_att_o"),
        "conv_w": g_cw, "conv_b": g_cb, "w_rg_a": blocks(g_wa), "b_rg_a": g_ba,
        "w_rg_i": blocks(g_wi), "b_rg_i": g_bi, "lru_lambda": g_lam,
        "w_rec_o": _matmul(yrec, dyr, "tn", BF16, "g_w_rec_o"),
        "w_out": _matmul(mixed, dx1, "tn", BF16, "g_w_out"),
    })
    dqkv, gbias = _att_bwd(qkv, bias, datt, reduce_early(grads))
    dz = (dqkv, duy, dgg)
    g_w_in_t, g_b_in = _grad_w_in(dz, h)
    grads.update(w_in_t=g_w_in_t, b_in=g_b_in)
    grad_x, g_ln1 = _dh_norm1_bwd(dz, p["w_in_t"], x, p["ln1_g"], dx1, reduce_late(grads))
    grads.update(ln1_g=g_ln1, rpb=_rpb_fold(gbias))
    return loss8[0:1, 0:1], grad_x, grads


MESH_ID = pl.DeviceIdType.MESH
ANY = pl.BlockSpec(memory_space=pl.ANY)

CHAN_BLOCK_ROWS = 32
GATE_ROWS = 2 * 2 * N_REC_BLOCKS * REC_BLOCK * REC_BLOCK // (N_DEV * D)
SECTIONS = (("w_in_t", 704, D), ("w_rec_o", 128, D), ("w_out", 128, D), ("w_ff1_t", 512, D),
            ("w_ff2", 512, D), ("chan", CHAN_BLOCK_ROWS, D), ("w_att_o_t", 128, D_ATT),
            ("gates", GATE_ROWS, D))
N_SEC = len(SECTIONS)
N_CHAN_ROWS = 10
CHAN = (("conv_w", 4), ("b_rg_a", 2), ("b_rg_i", 2), ("lru_lambda", 2))


def _position():
    return lax.axis_index("x"), lax.axis_index("y"), lax.axis_index("c")


def _other_chips(x, y):
    return [(1 - x, y), (x, 1 - y), (1 - x, 1 - y)]


PASS_ON_IDS, PAIR_EARLY_ID, PAIR_LATE_ID, PAIR_FIRST_ID, SMALL_PASS_ON_ID = (1, 4), 2, 3, 5, 6


def _pair_handshake(x, y, c):
    barrier = pltpu.get_barrier_semaphore()
    pl.semaphore_signal(barrier, inc=1, device_id=(x, y, 1 - c), device_id_type=MESH_ID)
    pl.semaphore_wait(barrier, 1)


def _block_of(ref, dev, rows):
    return ref.at[pl.ds(pl.multiple_of(dev * rows, 16), rows)]


def _all_gather(shards, name):
    ns = len(shards)

    def body(*refs):
        x_refs, out_refs, done_ref = refs[:ns], refs[ns:2 * ns], refs[2 * ns]
        send_sems, recv_sems, local_sems = refs[2 * ns + 1:]
        done_ref[0, 0] = 0.0
        x, y, c = _position()
        me, sibling = (x, y, c), (x, y, 1 - c)
        x_nbr, y_nbr, diagonal = _other_chips(x, y)
        north = c == 1
        relay_from = (jnp.where(north, x_nbr[0], y_nbr[0]), jnp.where(north, x_nbr[1], y_nbr[1]))
        relay_to = (jnp.where(north, y_nbr[0], x_nbr[0]), jnp.where(north, y_nbr[1], x_nbr[1]))

        def rows(s, px, py, pc):
            return _block_of(out_refs[s], 4 * px + 2 * py + pc, shards[s].shape[0])

        def copy(k, s, block, to, from_shard=False):
            return pltpu.make_async_remote_copy(
                src_ref=x_refs[s] if from_shard else rows(s, *block), dst_ref=rows(s, *block),
                send_sem=send_sems.at[k * ns + s], recv_sem=recv_sems.at[k * ns + s],
                device_id=to, device_id_type=MESH_ID)

        sections = range(ns)
        mine = [pltpu.make_async_copy(x_refs[s], rows(s, *me), local_sems.at[s]) for s in sections]
        sent = [copy(k, s, me, to, True) for k, to in enumerate((sibling, (*x_nbr, c), (*y_nbr, c)))
                for s in sections]
        for cp in mine + sent:
            cp.start()
        for s in sections:
            copy(1, s, (*x_nbr, c), me).wait_recv()
            copy(2, s, (*y_nbr, c), me).wait_recv()
            sent += [copy(3, s, (*relay_from, c), (*relay_to, c)),
                     copy(4, s, (*x_nbr, c), sibling), copy(5, s, (*y_nbr, c), sibling)]
            for cp in sent[-3:]:
                cp.start()
        for s in sections:
            copy(3, s, (*diagonal, c), me).wait_recv()
            sent.append(copy(6, s, (*diagonal, c), sibling))
            sent[-1].start()
        for s in sections:
            copy(0, s, sibling, me).wait_recv()
            for k, chip in ((4, x_nbr), (5, y_nbr), (6, diagonal)):
                copy(k, s, (*chip, 1 - c), me).wait_recv()
        for cp in sent:
            cp.wait_send()
        for cp in mine:
            cp.wait()

    return pl.pallas_call(
        body, name=name,
        out_shape=tuple(jax.ShapeDtypeStruct((N_DEV * s.shape[0], s.shape[1]), s.dtype) for s in shards)
        + (jax.ShapeDtypeStruct((1, 1), F32),),
        in_specs=[ANY] * ns,
        out_specs=(ANY,) * ns + (pl.BlockSpec(memory_space=pltpu.SMEM),),
        scratch_shapes=[pltpu.SemaphoreType.DMA((7 * ns,)), pltpu.SemaphoreType.DMA((7 * ns,)),
                        pltpu.SemaphoreType.DMA((ns,))],
    )(*shards)


HBM = pl.BlockSpec(memory_space=pltpu.HBM)
SEM = pl.BlockSpec(memory_space=pltpu.SEMAPHORE)
EFFECT = pltpu.SideEffectType.DATAFLOW_SIDE_EFFECTING


def _in_hbm(a):
    return pltpu.with_memory_space_constraint(a, pltpu.HBM)


def _first_hop_copies(rows, which, x_refs, zones, send_sems, recv_sems):
    ns = len(rows)
    x, y, c = _position()
    targets = [(x, y, 1 - c)] + [(cx, cy, c) for cx, cy in _other_chips(x, y)]
    return [pltpu.make_async_remote_copy(
        src_ref=x_refs[i], dst_ref=_block_of(zones[i], 4 * x + 2 * y + c, rows[s]),
        send_sem=send_sems.at[k * ns + s], recv_sem=recv_sems.at[k * ns + s],
        device_id=to, device_id_type=MESH_ID)
        for k, to in enumerate(targets) for i, s in enumerate(which)]


def _after_all(arrays, name):
    def body(*refs):
        refs[-1][...] = jnp.zeros_like(refs[-1])

    return pl.pallas_call(
        body, name=name,
        out_shape=jax.ShapeDtypeStruct((8, LANES), F32),
        in_specs=[pl.BlockSpec(memory_space=pl.ANY)] * len(arrays),
        out_specs=pl.BlockSpec(memory_space=pltpu.VMEM),
    )(*arrays)


def _own_blocks_placed(shards, after, name):
    ns = len(shards)
    x, y, c = _position()
    me = jnp.reshape(4 * x + 2 * y + c, (1,)).astype(jnp.int32)
    if after is not None:
        shards = [*shards[:-1], shards[-1] + after.astype(shards[-1].dtype)]

    def body(me_ref, *refs):
        for s in range(ns):
            refs[ns + s][...] = refs[s][...]

    return pl.pallas_call(
        body, name=name,
        out_shape=tuple(jax.ShapeDtypeStruct((N_DEV * s.shape[0], s.shape[1]), s.dtype) for s in shards),
        grid_spec=pltpu.PrefetchScalarGridSpec(
            num_scalar_prefetch=1, grid=(1,),
            in_specs=[pl.BlockSpec(s.shape, lambda i, me: (0, 0)) for s in shards],
            out_specs=tuple(pl.BlockSpec(s.shape, lambda i, me: (me[0], 0)) for s in shards)),
        compiler_params=_params(dimension_semantics=("arbitrary",)),
    )(me, *shards)


def _gather_start(shards, after, name):
    ns = len(shards)
    zones = _own_blocks_placed(shards, after, name + "_own_blocks")

    def body(*refs):
        for cp in _first_hop_copies([s.shape[0] for s in shards], range(ns), refs[:ns], refs[ns:2 * ns],
                                    refs[2 * ns], refs[2 * ns + 1]):
            cp.start()
        refs[-1][...] = jnp.zeros_like(refs[-1])

    out = pl.pallas_call(
        body, name=name,
        out_shape=(pltpu.SemaphoreType.DMA((4 * ns,)), pltpu.SemaphoreType.DMA((4 * ns,)),
                   *[pltpu.HBM(a.shape, a.dtype) for a in (*shards, *zones)],
                   jax.ShapeDtypeStruct((8, LANES), F32)),
        in_specs=[HBM] * (2 * ns),
        out_specs=(SEM, SEM, *[HBM] * (2 * ns), pl.BlockSpec(memory_space=pltpu.VMEM)),
        input_output_aliases={i: 2 + i for i in range(2 * ns)},
        compiler_params=pltpu.CompilerParams(has_side_effects=EFFECT),
    )(*[_in_hbm(a) for a in shards], *[_in_hbm(a) for a in zones])
    return out[0], out[1], out[2:2 + ns], out[2 + ns:2 + 2 * ns], out[-1]


def _gather_wait(send_sems, recv_sems, rows, which, shards, zones, after, name):
    ns = len(shards)

    def body(*refs):
        for cp in _first_hop_copies(rows, which, refs[:ns], refs[ns:2 * ns], refs[2 * ns], refs[2 * ns + 1]):
            cp.wait_send()
            cp.wait_recv()

    out = pl.pallas_call(
        body, name=name,
        out_shape=tuple(pltpu.HBM(a.shape, a.dtype) for a in (*shards, *zones)),
        in_specs=[HBM] * (2 * ns) + [SEM, SEM, ANY],
        out_specs=(HBM,) * (2 * ns),
        input_output_aliases={i: i for i in range(2 * ns)},
        compiler_params=pltpu.CompilerParams(has_side_effects=EFFECT),
    )(*shards, *zones, send_sems, recv_sems, after)
    return out[:ns], out[ns:]


def _pass_on_copies(rows, in_refs, out_refs, send_sems, recv_sems):
    ns = len(rows)
    x, y, c = _position()
    return [pltpu.make_async_remote_copy(
        src_ref=_block_of(in_refs[s], 4 * cx + 2 * cy + c, rows[s]),
        dst_ref=_block_of(out_refs[s], 4 * cx + 2 * cy + c, rows[s]),
        send_sem=send_sems.at[j * ns + s], recv_sem=recv_sems.at[j * ns + s],
        device_id=(x, y, 1 - c), device_id_type=MESH_ID)
        for j, (cx, cy) in enumerate(_other_chips(x, y)) for s in range(ns)]


def _pass_on_start(rows, zones, barrier_id, name):
    ns = len(zones)

    def body(*refs):
        _pair_handshake(*_position())
        for cp in _pass_on_copies(rows, refs[:ns], refs[:ns], refs[ns], refs[ns + 1]):
            cp.start()
        refs[-1][...] = jnp.zeros_like(refs[-1])

    out = pl.pallas_call(
        body, name=name,
        out_shape=(pltpu.SemaphoreType.DMA((3 * ns,)), pltpu.SemaphoreType.DMA((3 * ns,)),
                   *[pltpu.HBM(z.shape, z.dtype) for z in zones], jax.ShapeDtypeStruct((8, LANES), F32)),
        in_specs=[HBM] * ns,
        out_specs=(SEM, SEM, *[HBM] * ns, pl.BlockSpec(memory_space=pltpu.VMEM)),
        input_output_aliases={i: 2 + i for i in range(ns)},
        compiler_params=pltpu.CompilerParams(has_side_effects=EFFECT, collective_id=barrier_id),
    )(*[_in_hbm(z) for z in zones])
    return out[0], out[1], out[2:2 + ns], out[-1]


def _pass_on_wait(rows, send_sems, recv_sems, zones, after, name):
    ns = len(zones)

    def body(*refs):
        for cp in _pass_on_copies(rows, refs[:ns], refs[:ns], refs[ns], refs[ns + 1]):
            cp.wait_send()
            cp.wait_recv()

    return pl.pallas_call(
        body, name=name,
        out_shape=tuple(pltpu.HBM(z.shape, z.dtype) for z in zones),
        in_specs=[HBM] * ns + [SEM, SEM, ANY],
        out_specs=(HBM,) * ns,
        input_output_aliases={i: i for i in range(ns)},
        compiler_params=pltpu.CompilerParams(has_side_effects=EFFECT),
    )(*zones, send_sems, recv_sems, after)


def _gather_pass_on(rows, zones, barrier_id, name):
    ns = len(zones)

    def body(*refs):
        _pair_handshake(*_position())
        copies = _pass_on_copies(rows, refs[:ns], refs[ns:2 * ns], *refs[2 * ns:])
        for cp in copies:
            cp.start()
        for cp in copies:
            cp.wait_recv()
        for cp in copies:
            cp.wait_send()

    return pl.pallas_call(
        body, name=name,
        out_shape=tuple(jax.ShapeDtypeStruct(z.shape, z.dtype) for z in zones),
        in_specs=[ANY] * ns, out_specs=(ANY,) * ns,
        input_output_aliases={i: i for i in range(ns)},
        scratch_shapes=[pltpu.SemaphoreType.DMA((3 * ns,)), pltpu.SemaphoreType.DMA((3 * ns,))],
        compiler_params=pltpu.CompilerParams(collective_id=barrier_id),
    )(*zones)


def _pair_copies(sections, g_refs, land, send_sems, recv_sems):
    ns = len(sections)
    x, y, c = _position()
    return [pltpu.make_async_remote_copy(
        src_ref=_block_of(g_refs[s], 2 * k + 1 - c, rows), dst_ref=land[s].at[k],
        send_sem=send_sems.at[k * ns + s], recv_sem=recv_sems.at[k * ns + s],
        device_id=(x, y, 1 - c), device_id_type=MESH_ID)
        for k in range(N_CHIPS) for s, (_, rows, _) in enumerate(sections)]


def _pair_exchange_start(sections, grads, barrier_id, name):
    ns = len(sections)

    def body(*refs):
        _pair_handshake(*_position())
        for cp in _pair_copies(sections, refs[:ns], refs[ns:2 * ns], refs[2 * ns], refs[2 * ns + 1]):
            cp.start()
        refs[-1][...] = jnp.zeros_like(refs[-1])

    zones = [lax.empty((N_CHIPS, rows, cols), BF16) for _, rows, cols in sections]
    n = N_CHIPS * ns
    out = pl.pallas_call(
        body, name=name,
        out_shape=(pltpu.SemaphoreType.DMA((n,)), pltpu.SemaphoreType.DMA((n,)),
                   *[pltpu.HBM(a.shape, a.dtype) for a in (*grads, *zones)],
                   jax.ShapeDtypeStruct((8, LANES), F32)),
        in_specs=[HBM] * (2 * ns),
        out_specs=(SEM, SEM, *[HBM] * (2 * ns), pl.BlockSpec(memory_space=pltpu.VMEM)),
        input_output_aliases={i: 2 + i for i in range(2 * ns)},
        compiler_params=pltpu.CompilerParams(has_side_effects=EFFECT, collective_id=barrier_id),
    )(*[_in_hbm(a) for a in grads], *[_in_hbm(a) for a in zones])
    return out[0], out[1], out[2:2 + ns], out[2 + ns:2 + 2 * ns], out[-1]


def _pair_exchange_wait(sections, send_sems, recv_sems, grads, zones, after, name):
    ns = len(sections)

    def body(*refs):
        for cp in _pair_copies(sections, refs[:ns], refs[ns:2 * ns], refs[2 * ns], refs[2 * ns + 1]):
            cp.wait_send()
            cp.wait_recv()

    out = pl.pallas_call(
        body, name=name,
        out_shape=tuple(pltpu.HBM(a.shape, a.dtype) for a in (*grads, *zones)),
        in_specs=[HBM] * (2 * ns) + [SEM, SEM, ANY],
        out_specs=(HBM,) * (2 * ns),
        input_output_aliases={i: i for i in range(2 * ns)},
        compiler_params=pltpu.CompilerParams(has_side_effects=EFFECT),
    )(*grads, *zones, send_sems, recv_sems, after)
    return out[:ns], out[ns:]


def _pair_add(sections, grads, got, core, name):
    ns = len(sections)

    def body(core_ref, *refs):
        g_refs, got_refs, p_refs = refs[:ns], refs[ns:2 * ns], refs[2 * ns:]
        for s in range(ns):
            p_refs[s][0] = (g_refs[s][...].astype(F32) + got_refs[s][0].astype(F32)).astype(BF16)

    slot = [pl.BlockSpec((1, rows, cols), lambda k, c: (k, 0, 0)) for _, rows, cols in sections]
    return pl.pallas_call(
        body, name=name,
        out_shape=tuple(jax.ShapeDtypeStruct((N_CHIPS, rows, cols), BF16) for _, rows, cols in sections),
        grid_spec=pltpu.PrefetchScalarGridSpec(
            num_scalar_prefetch=1, grid=(N_CHIPS,),
            in_specs=[pl.BlockSpec((rows, cols), lambda k, c: (2 * k + c[0], 0)) for _, rows, cols in sections]
            + slot,
            out_specs=tuple(slot)),
        compiler_params=_params(dimension_semantics=("parallel",)),
    )(core, *grads, *got)


def _chip_copies(sections, p_refs, land, send_sems, recv_sems):
    ns = len(sections)
    x, y, c = _position()
    return [pltpu.make_async_remote_copy(
        src_ref=p_refs[s].at[2 * cx + cy], dst_ref=land[s].at[j],
        send_sem=send_sems.at[j * ns + s], recv_sem=recv_sems.at[j * ns + s],
        device_id=(cx, cy, c), device_id_type=MESH_ID)
        for j, (cx, cy) in enumerate(_other_chips(x, y)) for s in range(ns)]


def _chip_exchange(sections, parts, name):
    ns = len(sections)

    def body(*refs):
        copies = _chip_copies(sections, refs[:ns], refs[ns:2 * ns], *refs[2 * ns:])
        for cp in copies:
            cp.start()
        for cp in copies:
            cp.wait_recv()
        for cp in copies:
            cp.wait_send()

    n = 3 * ns
    return pl.pallas_call(
        body, name=name,
        out_shape=tuple(jax.ShapeDtypeStruct((3, rows, cols), BF16) for _, rows, cols in sections),
        in_specs=[ANY] * ns, out_specs=(ANY,) * ns,
        scratch_shapes=[pltpu.SemaphoreType.DMA((n,)), pltpu.SemaphoreType.DMA((n,))],
    )(*parts)


def _chip_exchange_start(sections, parts, name):
    ns = len(sections)

    def body(*refs):
        p_refs, land = refs[:ns], refs[ns:2 * ns]
        send_sems, recv_sems = refs[2 * ns], refs[2 * ns + 1]
        token = refs[-1]
        for cp in _chip_copies(sections, p_refs, land, send_sems, recv_sems):
            cp.start()
        token[...] = jnp.zeros_like(token)

    zones = [lax.empty((3, rows, cols), BF16) for _, rows, cols in sections]
    out = pl.pallas_call(
        body, name=name,
        out_shape=(pltpu.SemaphoreType.DMA((3 * ns,)), pltpu.SemaphoreType.DMA((3 * ns,)),
                   *[pltpu.HBM(a.shape, a.dtype) for a in parts], *[pltpu.HBM(a.shape, a.dtype) for a in zones],
                   jax.ShapeDtypeStruct((8, LANES), F32)),
        in_specs=[HBM] * (2 * ns),
        out_specs=(SEM, SEM, *[HBM] * (2 * ns), pl.BlockSpec(memory_space=pltpu.VMEM)),
        input_output_aliases={i: 2 + i for i in range(2 * ns)},
        compiler_params=pltpu.CompilerParams(has_side_effects=EFFECT),
    )(*[_in_hbm(a) for a in parts], *[_in_hbm(a) for a in zones])
    return out[0], out[1], out[2:2 + ns], out[2 + ns:2 + 2 * ns], out[-1]


def _chip_exchange_wait(sections, send_sems, recv_sems, parts, zones, after, name):
    ns = len(sections)

    def body(*refs):
        p_refs, land = refs[:ns], refs[ns:2 * ns]
        for cp in _chip_copies(sections, p_refs, land, refs[2 * ns], refs[2 * ns + 1]):
            cp.wait_send()
            cp.wait_recv()

    out = pl.pallas_call(
        body, name=name,
        out_shape=tuple(pltpu.HBM(a.shape, a.dtype) for a in (*parts, *zones)),
        in_specs=[HBM] * (2 * ns) + [SEM, SEM, ANY],
        out_specs=(HBM,) * (2 * ns),
        input_output_aliases={i: i for i in range(2 * ns)},
        compiler_params=pltpu.CompilerParams(has_side_effects=EFFECT),
    )(*parts, *zones, send_sems, recv_sems, after)
    return out[:ns], out[ns:]


def _grad_finish(sections, parts, far, chip, name):
    ns = len(sections)

    def body(chip_ref, *refs):
        p_refs, b_refs, g_refs = refs[:ns], refs[ns:2 * ns], refs[2 * ns:]
        for s in range(ns):
            g = p_refs[s][0].astype(F32)
            for j in range(3):
                g = g + b_refs[s][j].astype(F32)
            g_refs[s][...] = g

    half = [(rows // 2, cols) for _, rows, cols in sections]
    return pl.pallas_call(
        body, name=name,
        out_shape=tuple(jax.ShapeDtypeStruct((rows, cols), F32) for _, rows, cols in sections),
        grid_spec=pltpu.PrefetchScalarGridSpec(
            num_scalar_prefetch=1, grid=(2,),
            in_specs=[pl.BlockSpec((1, r, c), lambda i, chip: (chip[0], i, 0)) for r, c in half]
            + [pl.BlockSpec((3, r, c), lambda i, chip: (0, i, 0)) for r, c in half],
            out_specs=tuple(pl.BlockSpec((r, c), lambda i, chip: (i, 0)) for r, c in half)),
        compiler_params=_params(dimension_semantics=("parallel",)),
    )(chip, *parts, *far)


def _sum_devices(parts, rows, name):
    cols = parts.shape[1]
    tr = rows // 2

    def body(*refs):
        s = refs[0][...].astype(F32)
        for d in range(1, N_DEV):
            s = s + refs[d][...].astype(F32)
        refs[N_DEV][...] = s

    return pl.pallas_call(
        body, name=name,
        out_shape=jax.ShapeDtypeStruct((rows, cols), F32),
        grid=(2,),
        in_specs=[pl.BlockSpec((tr, cols), lambda i, d=d: (2 * d + i, 0)) for d in range(N_DEV)],
        out_specs=pl.BlockSpec((tr, cols), lambda i: (i, 0)),
        compiler_params=_params(dimension_semantics=("parallel",)),
    )(*([parts] * N_DEV))


def _adamw_step(w_ref, g_ref, m_ref, v_ref, d_ref, nm_ref, nv_ref):
    c1 = 1.0 / (1.0 - ADAM_B1 ** ADAM_STEP)
    c2 = 1.0 / (1.0 - ADAM_B2 ** ADAM_STEP)
    gv = g_ref[...]
    nm = ADAM_B1 * m_ref[...] + (1.0 - ADAM_B1) * gv
    nv = ADAM_B2 * v_ref[...] + (1.0 - ADAM_B2) * (gv * gv)
    nm_ref[...] = nm
    nv_ref[...] = nv
    d_ref[...] = (-ADAM_LR) * ((nm * c1) / (jnp.sqrt(nv * c2) + ADAM_EPS) + ADAM_WD * w_ref[...])


def _adamw_small(params, name):
    n = len(params)

    def body(*refs):
        for k in range(n):
            _adamw_step(*refs[4 * k:4 * k + 4], *refs[4 * n + 3 * k:4 * n + 3 * k + 3])

    out = pl.pallas_call(
        body, name=name,
        out_shape=tuple(jax.ShapeDtypeStruct(p[0].shape, F32) for p in params for _ in range(3)),
    )(*[a for p in params for a in p])
    return [out[3 * k:3 * k + 3] for k in range(n)]


def _adamw(w, g, m, v, name, after=None):
    rows, cols = w.shape
    tr = rows
    while tr * cols * 4 > (1 << 20) and tr % 16 == 0:
        tr //= 2
    tokens = [] if after is None else [after]

    def body(*refs):
        _adamw_step(*refs[:4], *refs[4 + len(tokens):])

    spec = pl.BlockSpec((tr, cols), lambda i: (i, 0))
    shape = jax.ShapeDtypeStruct((rows, cols), F32)
    return pl.pallas_call(
        body, name=name,
        out_shape=(shape, shape, shape),
        grid=(rows // tr,),
        in_specs=[spec] * 4 + [ANY] * len(tokens), out_specs=(spec,) * 3,
        compiler_params=_params(dimension_semantics=("parallel",)),
    )(w, g, m, v, *tokens)


NAMES = ("ln1_g", "w_in", "b_in", "rpb", "w_att_o", "conv_w", "conv_b", "w_rg_a", "b_rg_a", "w_rg_i",
         "b_rg_i", "lru_lambda", "w_rec_o", "w_out", "ln2_g", "w_ff1", "w_ff2", "lnf_g")
TRANSPOSED = {"w_in": "w_in_t", "w_att_o": "w_att_o_t", "w_ff1": "w_ff1_t"}
ROW_SHARDED = ("w_rec_o", "w_out", "w_ff2")
REPLICATED = (("ln1_g", (1, D)), ("b_in", (1, D_IN)), ("rpb", (N_HEADS * N_RPB_R, N_RPB_C)),
              ("conv_b", (1, D_REC)), ("w_rg_a", (2 * N_REC_BLOCKS * REC_BLOCK, REC_BLOCK)),
              ("w_rg_i", (2 * N_REC_BLOCKS * REC_BLOCK, REC_BLOCK)), ("ln2_g", (1, D)), ("lnf_g", (1, D)))
GATE_BLOCKS = ("w_rg_a", "w_rg_i")
SMALL_ROWS = 112


def _chan_bits(vectors):
    chan = jnp.concatenate(vectors, axis=0)
    bits = lax.bitcast_convert_type(chan, BF16).reshape(-1)
    return jnp.pad(bits, (0, CHAN_BLOCK_ROWS * D - bits.shape[0])).reshape(CHAN_BLOCK_ROWS, D)


def _chan_from_bits(gathered):
    bits = gathered.reshape(N_DEV, CHAN_BLOCK_ROWS * D)[:, :2 * N_CHAN_ROWS * LANES]
    chan = lax.bitcast_convert_type(bits.reshape(N_DEV, N_CHAN_ROWS, LANES, 2), F32)
    return chan.transpose(1, 0, 2).reshape(N_CHAN_ROWS, D)


def kernel(x, ln1_g, w_in, b_in, rpb, w_att_o, conv_w, conv_b, w_rg_a, b_rg_a, w_rg_i, b_rg_i, lru_lambda, w_rec_o, w_out, ln2_g, w_ff1, w_ff2, lnf_g, loss_target, m_ln1_g, m_w_in, m_b_in, m_rpb, m_w_att_o, m_conv_w, m_conv_b, m_w_rg_a, m_b_rg_a, m_w_rg_i, m_b_rg_i, m_lru_lambda, m_w_rec_o, m_w_out, m_ln2_g, m_w_ff1, m_w_ff2, m_lnf_g, v_ln1_g, v_w_in, v_b_in, v_rpb, v_w_att_o, v_conv_w, v_conv_b, v_w_rg_a, v_b_rg_a, v_w_rg_i, v_b_rg_i, v_lru_lambda, v_w_rec_o, v_w_out, v_ln2_g, v_w_ff1, v_w_ff2, v_lnf_g):
    w = dict(zip(NAMES, (ln1_g, w_in, b_in, rpb, w_att_o, conv_w, conv_b, w_rg_a, b_rg_a, w_rg_i,
                         b_rg_i, lru_lambda, w_rec_o, w_out, ln2_g, w_ff1, w_ff2, lnf_g)))
    m = dict(zip(NAMES, (m_ln1_g, m_w_in, m_b_in, m_rpb, m_w_att_o, m_conv_w, m_conv_b, m_w_rg_a,
                         m_b_rg_a, m_w_rg_i, m_b_rg_i, m_lru_lambda, m_w_rec_o, m_w_out, m_ln2_g,
                         m_w_ff1, m_w_ff2, m_lnf_g)))
    v = dict(zip(NAMES, (v_ln1_g, v_w_in, v_b_in, v_rpb, v_w_att_o, v_conv_w, v_conv_b, v_w_rg_a,
                         v_b_rg_a, v_w_rg_i, v_b_rg_i, v_lru_lambda, v_w_rec_o, v_w_out, v_ln2_g,
                         v_w_ff1, v_w_ff2, v_lnf_g)))
    xi, yi, ci = _position()

    shard = {t: w[n][0].T.astype(BF16) for n, t in TRANSPOSED.items()}
    shard.update({n: w[n][0].astype(BF16) for n in ROW_SHARDED})
    shard["chan"] = _chan_bits([w[n][0] for n, _ in CHAN])
    first, later = ("w_in_t", "chan"), ("w_rec_o", "w_out", "w_att_o_t", "w_ff1_t", "w_ff2")
    *gathered, done = _all_gather([shard[n] for n in first], "weight_all_gather")
    p = dict(zip(first, gathered))
    send_sems, recv_sems, sent, zones, token = _gather_start([shard[n] for n in later], done,
                                                             "weight_gather_start")

    stages = (("w_rec_o", "w_out", "w_att_o_t"), ("w_ff1_t", "w_ff2"))
    passing = {}

    def late_weights(after, stage):
        which = [later.index(n) for n in stages[stage]]
        _, arrived = _gather_wait(
            send_sems, recv_sems, [shard[n].shape[0] for n in later], which, [sent[i] for i in which],
            [zones[i] for i in which], after, "weight_gather_wait_%d" % stage)
        passing[stage] = _pass_on_start(
            [shard[n].shape[0] for n in stages[stage]], arrived,
            PASS_ON_IDS[stage], "weight_pass_on_start_%d" % stage)
        return passing[stage][-1]

    def late_weights_ready(after, stage):
        pass_send_sems, pass_recv_sems, pass_zones, _ = passing[stage]
        return dict(zip(stages[stage], _pass_on_wait(
            [shard[n].shape[0] for n in stages[stage]], pass_send_sems, pass_recv_sems, pass_zones, after,
            "weight_pass_on_wait_%d" % stage)))

    chan = _chan_from_bits(p.pop("chan"))
    r0 = 0
    for n, rows in CHAN:
        p[n] = chan[r0:r0 + rows]
        r0 += rows
    p.update(ln1_g=w["ln1_g"], b_in=w["b_in"] + token[0, 0], rpb=w["rpb"][0], conv_b=w["conv_b"],
             w_rg_a=w["w_rg_a"][0], w_rg_i=w["w_rg_i"][0], ln2_g=w["ln2_g"],
             lnf_g=w["lnf_g"].reshape(1, D))

    core = jnp.reshape(ci, (1,)).astype(jnp.int32)
    chip = jnp.reshape(2 * xi + yi, (1,)).astype(jnp.int32)
    first_sections = tuple(s for s in SECTIONS if s[0] in ("w_ff1_t", "w_ff2"))
    late_sections = SECTIONS[:1]
    early_sections = tuple(s for s in SECTIONS[1:] if s not in first_sections)
    in_flight = {}

    def pair_sum_and_send(group, sections, after):
        send_sems, recv_sems, sect, zones, _ = in_flight["pair_" + group]
        sect, got = _pair_exchange_wait(sections, send_sems, recv_sems, sect, zones, after,
                                        "grad_pair_exchange_wait_" + group)
        parts = _pair_add(sections, sect, got, core, "grad_pair_add_" + group)
        in_flight[group] = _chip_exchange_start(sections, parts, "grad_chip_exchange_start_" + group)
        return in_flight[group][-1]

    def pair_exchange_at_once(group, sections, grads, barrier_id):
        in_flight["pair_" + group] = _pair_exchange_start(
            sections, [grads[n] for n, _, _ in sections], barrier_id, "grad_pair_exchange_start_" + group)
        return pair_sum_and_send(group, sections, in_flight["pair_" + group][-1])

    def reduce_first(grads, after):
        if grads is None:
            return pair_sum_and_send("first", first_sections, after)
        in_flight["pair_first"] = _pair_exchange_start(
            first_sections, [grads[n] for n, _, _ in first_sections], PAIR_FIRST_ID,
            "grad_pair_exchange_start_first")
        return in_flight["pair_first"][-1]

    def reduce_early(grads):
        chan_g = jnp.concatenate([grads[n] for n, _ in CHAN], axis=0)
        chan_g = chan_g.reshape(N_CHAN_ROWS, N_DEV, LANES).transpose(1, 0, 2).astype(BF16)
        chan_g = jnp.pad(chan_g.reshape(N_DEV, -1), ((0, 0), (0, CHAN_BLOCK_ROWS * D - N_CHAN_ROWS * LANES)))
        grads["chan"] = chan_g.reshape(N_DEV * CHAN_BLOCK_ROWS, D)
        grads["gates"] = jnp.concatenate([grads[n].reshape(-1, D) for n in GATE_BLOCKS], axis=0).astype(BF16)
        return pair_exchange_at_once("early", early_sections, grads, PAIR_EARLY_ID)[0, 0]

    def finish(group, sections, after, name):
        send_sems, recv_sems, parts, zones, _ = in_flight[group]
        parts, far = _chip_exchange_wait(sections, send_sems, recv_sems, parts, zones, after,
                                         "grad_chip_exchange_wait_" + name)
        return dict(zip((n for n, _, _ in sections),
                        _grad_finish(sections, parts, far, chip, "grad_finish_" + name)))

    summed = {}

    def reduce_late(grads):
        in_flight["pair_late"] = _pair_exchange_start(
            late_sections, [grads[n] for n, _, _ in late_sections], PAIR_LATE_ID,
            "grad_pair_exchange_start_late")
        summed.update(finish("first", first_sections, in_flight["pair_late"][-1], "first"))
        summed.update(finish("early", early_sections, summed["w_ff2"], "early"))
        return pair_sum_and_send("late", late_sections, summed["gates"])

    loss_part, grad_x, grads = _local_step(x[0], loss_target[0], p, late_weights, late_weights_ready,
                                           reduce_first, reduce_early, reduce_late)

    flat = jnp.concatenate([grads[n].reshape(-1) for n, _ in REPLICATED if n not in GATE_BLOCKS]
                           + [loss_part.reshape(-1)])
    n_small = flat.shape[0]
    flat = jnp.pad(flat, (0, SMALL_ROWS * LANES - n_small)).reshape(SMALL_ROWS, LANES)
    *small_gather, small_started = _gather_start([flat, summed["gates"]], None, "small_grad_gather_start")

    g, delta, new_m, new_v = {}, {}, {}, {}

    def update(n, g2, shape2, after=None):
        d2, m2, v2 = _adamw(w[n].reshape(shape2), g2, m[n].reshape(shape2), v[n].reshape(shape2),
                            "adamw_" + n, after)
        g[n], delta[n], new_m[n], new_v[n] = (a.reshape(w[n].shape) for a in (g2, d2, m2, v2))

    for n in ROW_SHARDED:
        update(n, summed[n], summed[n].shape, small_started)
    for n, t in TRANSPOSED.items():
        if t in summed:
            update(n, summed[t].T, summed[t].shape[::-1], small_started)

    small_rows = [SMALL_ROWS, GATE_ROWS]
    _, small_zones = _gather_wait(
        *small_gather[:2], small_rows, range(2), *small_gather[2:],
        _after_all(list(delta.values()), "sharded_updates_done"), "small_grad_gather_wait")
    small_parts, gate_sum = _gather_pass_on(small_rows, small_zones, SMALL_PASS_ON_ID,
                                            "small_grad_gather_pass_on")
    small = _sum_devices(small_parts, SMALL_ROWS, "small_grad_sum").reshape(-1)
    loss = small[n_small - 1]

    small_params = []
    o = 0
    for n, shape2 in REPLICATED:
        if n in GATE_BLOCKS:
            k, rows = GATE_BLOCKS.index(n), gate_sum.shape[0] // len(GATE_BLOCKS)
            update(n, gate_sum[k * rows:(k + 1) * rows].reshape(shape2), shape2)
        else:
            size = shape2[0] * shape2[1]
            small_params.append((n, small[o:o + size].reshape(shape2), shape2))
            o += size
    chan_back = summed["chan"].reshape(-1)[:N_CHAN_ROWS * LANES].reshape(N_CHAN_ROWS, LANES)
    r0 = 0
    for n, rows in CHAN:
        small_params.append((n, chan_back[r0:r0 + rows], (rows, LANES)))
        r0 += rows
    results = _adamw_small([(w[n].reshape(s2), g2, m[n].reshape(s2), v[n].reshape(s2))
                            for n, g2, s2 in small_params], "adamw_vectors")
    for (n, g2, _), (d2, m2, v2) in zip(small_params, results):
        g[n], delta[n], new_m[n], new_v[n] = (a.reshape(w[n].shape) for a in (g2, d2, m2, v2))

    summed = finish("late", late_sections, _after_all(list(delta.values()), "updates_done"), "late")
    g_t = summed["w_in_t"]
    results = _adamw(w["w_in"][0].T, g_t, m["w_in"][0].T, v["w_in"][0].T, "adamw_w_in")
    g["w_in"], delta["w_in"], new_m["w_in"], new_v["w_in"] = (a.T[None] for a in (g_t, *results))

    return (loss, grad_x[None], *[g[n] for n in NAMES], *[delta[n] for n in NAMES],
            *[new_m[n] for n in NAMES], *[new_v[n] for n in NAMES])
```

```python
import math

import numpy as np
import jax
import jax.numpy as jnp
from jax import lax
from jax.experimental import pallas as pl
from jax.experimental.pallas import tpu as pltpu

F32 = jnp.float32
BF16 = jnp.bfloat16

T = 2048
D = 1024
D_ATT = 512
D_REC = 1024
D_FF = 4096
D_IN = 5632
N_HEADS = 8
DH = 64
GRID_W = 64
ROWS = T // GRID_W
WIN_H = 8
WIN_W = 16
KWIN = WIN_H * GRID_W
N_RPB_R = 2 * WIN_H - 1
N_RPB_C = 2 * WIN_W - 1
N_REC_BLOCKS = 16
REC_BLOCK = 64
CG = 128
N_CG = D_REC // CG
LRU_C = 8.0
EPS = 1e-6
N_DEV = 8
N_CHIPS = 4
LANES = 128

ADAM_LR = 0.001
ADAM_B1 = 0.9
ADAM_B2 = 0.999
ADAM_EPS = 1e-08
ADAM_WD = 0.01
ADAM_STEP = 10

MESH_AXES = ("x", "y", "c")
VMEM_LIMIT = 56 * 1024 * 1024

TILE = 512
DZ_ARRAYS = ((0, 3, 1), (3, 4, 2), (7, 4, 2))
N_DZ_TILES = D_IN // TILE


def _params(**kw):
    return pltpu.CompilerParams(vmem_limit_bytes=VMEM_LIMIT, **kw)


HG = 4
HQ = HG * GRID_W
HC = HG * DH


def _att_tables():
    rq = np.arange(GRID_W)
    kc = np.arange(KWIN) % GRID_W
    win_start = np.clip(rq - WIN_W // 2, 0, GRID_W - WIN_W)
    valid = (kc[None, :] >= win_start[:, None]) & (kc[None, :] < win_start[:, None] + WIN_W)
    same_head = (np.arange(HQ)[:, None] // GRID_W) == (np.arange(HC)[None, :] // DH)
    return valid.astype(np.float32), same_head.astype(np.float32)


def _pair_mask():
    half = np.arange(2 * DH) // DH
    return (half[:, None] == half[None, :]).astype(np.float32)


def _dup_table():
    return np.concatenate([np.eye(REC_BLOCK, dtype=np.float32)] * 2, axis=1)


def _sigmoid(x):
    return 0.5 * jnp.tanh(0.5 * x) + 0.5


def _softplus(x):
    return jnp.maximum(x, 0.0) + jnp.log(1.0 + jnp.exp(-jnp.abs(x)))


def _one_minus_square(log_a, a):
    x = 2.0 * log_a
    series = -x * (1.0 + x * (0.5 + x * (1.0 / 6.0)))
    return jnp.where(x > -0.02, series, 1.0 - a * a)


_GELU_C = math.sqrt(2.0 / math.pi)


def _gelu_and_grad(x):
    x2 = x * x
    inner = _GELU_C * (x + 0.044715 * x * x2)
    t = jnp.tanh(inner)
    g = 0.5 * x * (1.0 + t)
    dg = 0.5 * (1.0 + t) + 0.5 * x * (1.0 - t * t) * _GELU_C * (1.0 + 3.0 * 0.044715 * x2)
    return g, dg


def _dot(a, b):
    return jnp.dot(a, b, preferred_element_type=F32)


def _dot_nt(a, b):
    return lax.dot_general(a, b, (((1,), (1,)), ((), ())), preferred_element_type=F32)


def _dot_tn(a, b):
    return lax.dot_general(a, b, (((0,), (0,)), ((), ())), preferred_element_type=F32)


def _dot_exact(a, b):
    return jnp.dot(a, b, precision=lax.Precision.HIGHEST, preferred_element_type=F32)


def _shift_rows(x, s):
    n = x.shape[0]
    rows = lax.broadcasted_iota(jnp.int32, x.shape, 0)
    y = pltpu.roll(x, s % n, 0)
    if s > 0:
        return jnp.where(rows >= s, y, 0.0)
    return jnp.where(rows < n + s, y, 0.0)


def _rms_bwd(dh, xh, r, g):
    dxh = dh * g
    return r * (dxh - xh * jnp.mean(dxh * xh, axis=-1, keepdims=True))


def _matmul(a, b, mode, out_dtype, name, tm=512, tn=1024, tk=2048):
    if mode == "nn":
        (m, k), (k2, n) = a.shape, b.shape
    elif mode == "nt":
        (m, k), (n, k2) = a.shape, b.shape
    else:
        (k, m), (k2, n) = a.shape, b.shape
    assert k == k2
    tm, tn, tk = min(tm, m), min(tn, n), min(tk, k)
    assert m % tm == 0 and n % tn == 0 and k % tk == 0
    nk = k // tk
    dot = {"nn": _dot, "nt": _dot_nt, "tn": _dot_tn}[mode]

    def body(a_ref, b_ref, o_ref, acc):
        kk = pl.program_id(2)
        part = dot(a_ref[...].astype(BF16), b_ref[...].astype(BF16))
        if nk == 1:
            o_ref[...] = part.astype(out_dtype)
            return

        @pl.when(kk == 0)
        def _():
            acc[...] = part

        @pl.when(kk > 0)
        def _():
            acc[...] += part

        @pl.when(kk == nk - 1)
        def _():
            o_ref[...] = acc[...].astype(out_dtype)

    if mode == "tn":
        a_spec = pl.BlockSpec((tk, tm), lambda i, j, kk: (kk, i))
    else:
        a_spec = pl.BlockSpec((tm, tk), lambda i, j, kk: (i, kk))
    if mode == "nt":
        b_spec = pl.BlockSpec((tn, tk), lambda i, j, kk: (j, kk))
    else:
        b_spec = pl.BlockSpec((tk, tn), lambda i, j, kk: (kk, j))
    return pl.pallas_call(
        body, name=name,
        out_shape=jax.ShapeDtypeStruct((m, n), out_dtype),
        grid=(m // tm, n // tn, nk),
        in_specs=[a_spec, b_spec],
        out_specs=pl.BlockSpec((tm, tn), lambda i, j, kk: (i, j)),
        scratch_shapes=[pltpu.VMEM((tm, tn) if nk > 1 else (8, LANES), F32)],
        compiler_params=_params(dimension_semantics=("parallel", "parallel", "arbitrary")),
    )(a, b)


def _in_proj(x, g1, w_in_t, b_in, after):
    tm = 512

    def body(x_ref, g_ref, w_hbm, b_ref, after_ref, qkv_ref, uy_ref, gg_ref, h_ref, w):
        @pl.when(pl.program_id(0) == 0)
        def _():
            pltpu.sync_copy(w_hbm, w)

        xv = x_ref[...]
        r = lax.rsqrt(jnp.mean(xv * xv, axis=-1, keepdims=True) + EPS)
        h = ((xv * r) * g_ref[...]).astype(BF16)
        h_ref[...] = h
        row0 = 0
        for ref in (qkv_ref, uy_ref, gg_ref):
            for c0 in range(0, ref.shape[1], TILE):
                z = _dot_nt(h, w[row0:row0 + TILE, :]) + b_ref[:, row0:row0 + TILE]
                ref[:, c0:c0 + TILE] = z.astype(ref.dtype)
                row0 += TILE

    tok = lambda width: pl.BlockSpec((tm, width), lambda i: (i, 0))
    return pl.pallas_call(
        body, name="in_proj",
        out_shape=(jax.ShapeDtypeStruct((T, 3 * D_ATT), BF16),
                   jax.ShapeDtypeStruct((T, 2 * D_REC), F32),
                   jax.ShapeDtypeStruct((T, 2 * D), F32),
                   jax.ShapeDtypeStruct((T, D), BF16)),
        grid=(T // tm,),
        in_specs=[tok(D), pl.BlockSpec((1, D), lambda i: (0, 0)), pl.BlockSpec(memory_space=pl.ANY),
                  pl.BlockSpec((1, D_IN), lambda i: (0, 0)), pl.BlockSpec(memory_space=pl.ANY)],
        out_specs=(tok(3 * D_ATT), tok(2 * D_REC), tok(2 * D), tok(D)),
        scratch_shapes=[pltpu.VMEM((D_IN, D), BF16)],
        compiler_params=_params(dimension_semantics=("arbitrary",)),
    )(x, g1, w_in_t, b_in, after)


def _dz_specs(rows, tile_of, row_of):
    def spec(off, n, per_plane):
        def index(*ids):
            t = jnp.clip(tile_of(*ids) - off, 0, n - 1)
            return (t // per_plane, row_of(*ids), t % per_plane)
        return pl.BlockSpec((1, rows, TILE), index)
    return [spec(off, n, per) for off, n, per in DZ_ARRAYS]


def _dh_norm1_bwd(dz, w_in_t, x, g1, dx1, after):
    tm = 512

    def body(dqkv_ref, duy_ref, dgg_ref, w_hbm, x_ref, g_ref, dx1_ref, after_ref, gx_ref, dg_ref, w):
        @pl.when(pl.program_id(0) == 0)
        def _():
            pltpu.sync_copy(w_hbm, w)
            dg_ref[...] = jnp.zeros_like(dg_ref)

        dh, row0 = None, 0
        for ref in (dqkv_ref, duy_ref, dgg_ref):
            for plane in range(ref.shape[0]):
                cols = ref.shape[2]
                part = _dot(ref[plane], w[row0:row0 + cols, :])
                dh = part if dh is None else dh + part
                row0 += cols
        xv = x_ref[...]
        r = lax.rsqrt(jnp.mean(xv * xv, axis=-1, keepdims=True) + EPS)
        xh = xv * r
        dg_ref[...] += jnp.sum(dh * xh, axis=0, keepdims=True)
        gx_ref[...] = dx1_ref[...] + _rms_bwd(dh, xh, r, g_ref[...])

    tok = pl.BlockSpec((tm, D), lambda i: (i, 0))
    vec = pl.BlockSpec((1, D), lambda i: (0, 0))
    planes = lambda a: pl.BlockSpec((a.shape[0], tm, a.shape[2]), lambda i: (0, i, 0))
    return pl.pallas_call(
        body, name="dh_norm1_bwd",
        out_shape=(jax.ShapeDtypeStruct((T, D), F32), jax.ShapeDtypeStruct((1, D), F32)),
        grid=(T // tm,),
        in_specs=[planes(a) for a in dz] + [pl.BlockSpec(memory_space=pl.ANY), tok, vec, tok,
                                            pl.BlockSpec(memory_space=pl.ANY)],
        out_specs=(tok, vec),
        scratch_shapes=[pltpu.VMEM((D_IN, D), BF16)],
        compiler_params=_params(dimension_semantics=("arbitrary",)),
    )(*dz, w_in_t, x, g1, dx1, after)


def _grad_w_in(dz, h):
    def body(*refs):
        seg_refs = refs[:3]
        h_ref, gw_ref, gb_ref = refs[3:]
        j = pl.program_id(0)

        for s, (off, n, _) in enumerate(DZ_ARRAYS):
            @pl.when((j >= off) & (j < off + n))
            def _(s=s):
                a = seg_refs[s][0]
                gw_ref[...] = _dot_tn(a, h_ref[...]).astype(BF16)
                gb_ref[...] = jnp.sum(a.astype(F32), axis=0, keepdims=True)

    return pl.pallas_call(
        body, name="grad_w_in",
        out_shape=(jax.ShapeDtypeStruct((D_IN, D), BF16), jax.ShapeDtypeStruct((1, D_IN), F32)),
        grid=(N_DZ_TILES,),
        in_specs=_dz_specs(T, lambda j: j, lambda j: 0) + [pl.BlockSpec((T, D), lambda j: (0, 0))],
        out_specs=(pl.BlockSpec((TILE, D), lambda j: (j, 0)), pl.BlockSpec((1, TILE), lambda j: (0, j))),
        compiler_params=_params(dimension_semantics=("parallel",)),
    )(*dz, h)


def _rpb_rows(rpb):
    padded = jnp.pad(rpb, ((0, 0), (0, 0), (0, GRID_W - N_RPB_C)))
    rows = [padded[:, WIN_H - 1 - oi: 2 * WIN_H - 1 - oi].reshape(N_HEADS // HG, HG, KWIN)
            for oi in range(WIN_H)]
    return jnp.stack(rows, axis=0)


SKEW = KWIN - (WIN_W - 1)


MASKED = -1e30


def _bias_tiles(rows_ref, valid, bias_s):
    for oi in range(WIN_H):
        for hh in range(HG):
            row = jnp.broadcast_to(rows_ref[oi, 0, hh:hh + 1, :], (GRID_W, KWIN))
            tile = pltpu.roll(row, SKEW, 1, stride=1, stride_axis=0)
            bias_s[oi, hh * GRID_W:(hh + 1) * GRID_W, :] = jnp.where(valid, tile, MASKED)


def _bias_tile_grads(gb_s, flip, out_ref):
    for oi in range(WIN_H):
        for hh in range(HG):
            g = _dot_exact(flip, gb_s[oi, hh * GRID_W:(hh + 1) * GRID_W, :])
            back = pltpu.roll(g, KWIN - (GRID_W - WIN_W), 1, stride=1, stride_axis=0)
            out_ref[0, oi, hh:hh + 1, :] = jnp.sum(back, axis=0, keepdims=True)


def _rpb_fold(row_grads):
    g = row_grads.transpose(1, 0, 2, 3).reshape(WIN_H, N_HEADS, WIN_H, GRID_W)
    g = g.transpose(0, 2, 1, 3)

    def body(g_ref, o_ref):
        for dr in range(N_RPB_R):
            terms = [g_ref[oi, i] for oi in range(WIN_H) for i in range(WIN_H) if i - oi + WIN_H - 1 == dr]
            acc = terms[0]
            for term in terms[1:]:
                acc = acc + term
            o_ref[dr] = acc

    out = pl.pallas_call(
        body, name="rpb_fold",
        out_shape=jax.ShapeDtypeStruct((N_RPB_R, N_HEADS, GRID_W), F32),
    )(g)
    return out.transpose(1, 0, 2)[:, :, :N_RPB_C]


ATT_GROUPS = N_HEADS // HG
ATT_UNROLL = 8


def _stacked(rows64, same_head):
    return jnp.where(same_head, jnp.concatenate([rows64] * HG, axis=0), jnp.zeros((), BF16))


def _own_heads(stacked):
    head = lax.broadcasted_iota(jnp.int32, (GRID_W, HC), 1) // DH
    out = stacked[:GRID_W]
    for h in range(1, HG):
        out = jnp.where(head == h, stacked[h * GRID_W:(h + 1) * GRID_W], out)
    return out


def _att_scores(q_ref, k_ref, bias_ref, same_head, r):
    rs = jnp.clip(r - WIN_H // 2, 0, ROWS - WIN_H)
    oi = r - rs
    q0 = pl.multiple_of(r * GRID_W, GRID_W)
    k0 = pl.multiple_of(rs * GRID_W, GRID_W)
    q2 = _stacked(q_ref[pl.ds(q0, GRID_W), :] * (DH ** -0.5), same_head)
    kw = k_ref[pl.ds(k0, KWIN), :]
    s = _dot_nt(q2, kw) + bias_ref[oi]
    e = jnp.exp(s - jnp.max(s, axis=-1, keepdims=True))
    return e, 1.0 / jnp.sum(e, axis=-1, keepdims=True), q2, kw, q0, k0, oi


def _att_specs():
    col = lambda off: pl.BlockSpec((T, HC), lambda g: (0, g + off * ATT_GROUPS))
    tables = [pl.BlockSpec((WIN_H, 1, HG, KWIN), lambda g: (0, g, 0, 0)),
              pl.BlockSpec((GRID_W, KWIN), lambda g: (0, 0)),
              pl.BlockSpec((HQ, HC), lambda g: (0, 0))]
    return col, tables, pltpu.VMEM((WIN_H, HQ, KWIN), F32)


def _att_fwd(qkv, bias_rows, after):
    valid_np, same_head_np = _att_tables()

    def body(q_ref, k_ref, v_ref, rows_ref, valid_ref, head_ref, after_ref, o_ref, bias_s):
        same_head = head_ref[...] > 0.5
        _bias_tiles(rows_ref, valid_ref[...] > 0.5, bias_s)

        def row(r, carry):
            e, rl, _, _, q0, k0, _ = _att_scores(q_ref, k_ref, bias_s, same_head, r)
            o2 = _dot((e * rl).astype(BF16), v_ref[pl.ds(k0, KWIN), :])
            o_ref[pl.ds(q0, GRID_W), :] = _own_heads(o2).astype(BF16)
            return carry

        lax.fori_loop(0, ROWS, row, 0, unroll=ATT_UNROLL)

    col, tables, tiles = _att_specs()
    return pl.pallas_call(
        body, name="att_fwd",
        out_shape=jax.ShapeDtypeStruct((T, D_ATT), BF16),
        grid=(ATT_GROUPS,),
        in_specs=[col(0), col(1), col(2)] + tables + [pl.BlockSpec(memory_space=pl.ANY)],
        out_specs=col(0),
        scratch_shapes=[tiles],
        compiler_params=_params(dimension_semantics=("parallel",)),
    )(qkv, qkv, qkv, bias_rows, jnp.asarray(valid_np), jnp.asarray(same_head_np), after)


def _att_bwd(qkv, bias_rows, datt, after):
    valid_np, same_head_np = _att_tables()

    def body(q_ref, k_ref, v_ref, do_ref, rows_ref, valid_ref, head_ref, flip_ref, after_ref,
             dqkv_ref, grows_ref, dk_acc, dv_acc, bias_s, gb_s):
        same_head = head_ref[...] > 0.5
        dk_acc[...] = jnp.zeros_like(dk_acc)
        dv_acc[...] = jnp.zeros_like(dv_acc)
        gb_s[...] = jnp.zeros_like(gb_s)
        _bias_tiles(rows_ref, valid_ref[...] > 0.5, bias_s)

        def row(r, carry):
            e, rl, q2, kw, q0, k0, oi = _att_scores(q_ref, k_ref, bias_s, same_head, r)
            do2 = _stacked(do_ref[pl.ds(q0, GRID_W), :], same_head)
            vw = v_ref[pl.ds(k0, KWIN), :]
            p = e * rl
            dp = _dot_nt(do2, vw)
            ds = p * (dp - jnp.sum(dp * p, axis=-1, keepdims=True))
            p16 = p.astype(BF16)
            ds16 = ds.astype(BF16)
            dv_acc[pl.ds(k0, KWIN), :] += _dot_tn(p16, do2)
            dk_acc[pl.ds(k0, KWIN), :] += _dot_tn(ds16, q2)
            dq2 = _dot(ds16, kw) * (DH ** -0.5)
            dqkv_ref[0, pl.ds(q0, GRID_W), :] = _own_heads(dq2).astype(BF16)
            gb_s[oi] += ds
            return carry

        lax.fori_loop(0, ROWS, row, 0, unroll=ATT_UNROLL)
        dqkv_ref[1] = dk_acc[...].astype(BF16)
        dqkv_ref[2] = dv_acc[...].astype(BF16)
        _bias_tile_grads(gb_s, flip_ref[...], grows_ref)

    col, tables, tiles = _att_specs()
    return pl.pallas_call(
        body, name="att_bwd",
        out_shape=(jax.ShapeDtypeStruct((3, T, D_ATT), BF16),
                   jax.ShapeDtypeStruct((ATT_GROUPS, WIN_H, HG, KWIN), F32)),
        grid=(ATT_GROUPS,),
        in_specs=[col(0), col(1), col(2), col(0)] + tables + [pl.BlockSpec((GRID_W, GRID_W), lambda g: (0, 0)),
                                                              pl.BlockSpec(memory_space=pl.ANY)],
        out_specs=(pl.BlockSpec((3, T, HC), lambda g: (0, 0, g)),
                   pl.BlockSpec((1, WIN_H, HG, KWIN), lambda g: (g, 0, 0, 0))),
        scratch_shapes=[pltpu.VMEM((T, HC), F32), pltpu.VMEM((T, HC), F32), tiles, tiles],
        compiler_params=_params(dimension_semantics=("parallel",)),
    )(qkv, qkv, qkv, datt, bias_rows, jnp.asarray(valid_np), jnp.asarray(same_head_np),
      jnp.asarray(np.eye(GRID_W, dtype=np.float32)[::-1].copy()), after)


def _conv_taps(up):
    return (_shift_rows(up, 2), _shift_rows(up, 1), up, _shift_rows(up, -1))


def _pair_block_diag(w_pair, dup, same_half):
    return jnp.where(same_half, _dot(w_pair.astype(BF16), dup), 0.0).astype(BF16)


def _gates(u, u16, wa, ba, wi, bi, lam):
    r = _sigmoid(_dot(u16, wa) + ba)
    ig = _sigmoid(_dot(u16, wi) + bi)
    sp = _softplus(-lam)
    log_a = (-LRU_C) * r * sp
    a = jnp.exp(log_a)
    mult2 = jnp.maximum(_one_minus_square(log_a, a), 0.0)
    return r, ig, sp, a, jnp.sqrt(mult2), mult2


SCAN_BLOCKS = 8


def _scans(jobs):
    c = jobs[0][0].shape[1]
    nblk = T // 8
    rows = lax.broadcasted_iota(jnp.int32, (8, c), 0)

    def block(a, b, reverse):
        for s in (1, 2, 4):
            if reverse:
                keep = rows < 8 - s
                a_s = jnp.where(keep, pltpu.roll(a, 8 - s, 0), 1.0)
                b_s = jnp.where(keep, pltpu.roll(b, 8 - s, 0), 0.0)
            else:
                keep = rows >= s
                a_s = jnp.where(keep, pltpu.roll(a, s, 0), 1.0)
                b_s = jnp.where(keep, pltpu.roll(b, s, 0), 0.0)
            b = a * b_s + b
            a = a * a_s
        return a, b

    def step(i, carry):
        out = []
        for (a_ref, b_ref, h_ref, reverse), h_prev in zip(jobs, carry):
            for u in range(SCAN_BLOCKS):
                blk = i * SCAN_BLOCKS + u
                if reverse:
                    blk = nblk - 1 - blk
                t0 = pl.multiple_of(blk * 8, 8)
                a, b = block(a_ref[pl.ds(t0, 8), :], b_ref[pl.ds(t0, 8), :], reverse)
                h = a * h_prev + b
                h_ref[pl.ds(t0, 8), :] = h
                h_prev = jnp.broadcast_to(h[0:1] if reverse else h[7:8], (8, c))
            out.append(h_prev)
        return tuple(out)

    lax.fori_loop(0, nblk // SCAN_BLOCKS, step, tuple(jnp.zeros((8, c), F32) for _ in jobs))


def _rec_specs():
    tok = lambda off: pl.BlockSpec((T, CG), lambda g: (0, g + off))
    per_ch = lambda rows: pl.BlockSpec((rows, CG), lambda g: (0, g))
    wspec = pl.BlockSpec((2, 1, CG, REC_BLOCK), lambda g: (0, g, 0, 0))
    const = lambda shape: pl.BlockSpec(shape, lambda g: (0, 0))
    return tok, per_ch, wspec, const


def _rec_fwd(uy, conv_w, conv_b, w_a, b_a, w_i, b_i, lam):
    tok, per_ch, wspec, const = _rec_specs()

    def body(up_ref, yb_ref, cw_ref, cb_ref, wa_ref, ba_ref, wi_ref, bi_ref, lam_ref, dup_ref, half_ref,
             hf_ref, hb_ref, yrec_ref, am_ref, bx_f, bx_b):
        dup = dup_ref[...]
        same_half = half_ref[...] > 0.5
        taps = _conv_taps(up_ref[...])
        u = cb_ref[...]
        for j in range(4):
            u = u + taps[j] * cw_ref[j:j + 1, :]
        u16 = u.astype(BF16)
        for d, bx_s in enumerate((bx_f, bx_b)):
            wa = _pair_block_diag(wa_ref[d, 0], dup, same_half)
            wi = _pair_block_diag(wi_ref[d, 0], dup, same_half)
            _, ig, _, a, mult, _ = _gates(u, u16, wa, ba_ref[d:d + 1, :], wi, bi_ref[d:d + 1, :],
                                       lam_ref[d:d + 1, :])
            am_ref[2 * d] = a
            am_ref[2 * d + 1] = mult
            bx_s[...] = mult * (ig * u)
        _scans([(am_ref.at[0], bx_f, hf_ref, False), (am_ref.at[2], bx_b, hb_ref, True)])
        gelu, _ = _gelu_and_grad(yb_ref[...])
        yrec_ref[...] = ((hf_ref[...] + hb_ref[...]) * gelu).astype(BF16)

    return pl.pallas_call(
        body, name="rec_fwd",
        out_shape=(jax.ShapeDtypeStruct((T, D_REC), F32), jax.ShapeDtypeStruct((T, D_REC), F32),
                   jax.ShapeDtypeStruct((T, D_REC), BF16), jax.ShapeDtypeStruct((4, T, D_REC), F32)),
        grid=(N_CG,),
        in_specs=[tok(0), tok(N_CG), per_ch(4), per_ch(1), wspec, per_ch(2), wspec, per_ch(2), per_ch(2),
                  const((REC_BLOCK, CG)), const((CG, CG))],
        out_specs=(tok(0), tok(0), tok(0), pl.BlockSpec((4, T, CG), lambda g: (0, 0, g))),
        scratch_shapes=[pltpu.VMEM((T, CG), F32)] * 2,
        compiler_params=_params(dimension_semantics=("parallel",)),
    )(uy, uy, conv_w, conv_b, w_a, b_a, w_i, b_i, lam,
      jnp.asarray(_dup_table(), BF16), jnp.asarray(_pair_mask()))


def _rec_bwd(uy, hf, hb, am, dyrec, conv_w, conv_b, w_a, b_a, w_i, b_i, lam, after):
    tok, per_ch, wspec, const = _rec_specs()

    def body(up_ref, yb_ref, hf_ref, hb_ref, am_ref, dy_ref, cw_ref, cb_ref, wa_ref, ba_ref, wi_ref, bi_ref,
             lam_ref, dup_ref, dupt_ref, half_ref, after_ref,
             duy_ref, dcw_ref, dcb_ref, dwa_ref, dba_ref, dwi_ref, dbi_ref, dlam_ref,
             a_s0, a_s1, dh_s, g_s0, g_s1):
        dup = dup_ref[...]
        dup_t = dupt_ref[...]
        same_half = half_ref[...] > 0.5
        taps = _conv_taps(up_ref[...])
        u = cb_ref[...]
        for j in range(4):
            u = u + taps[j] * cw_ref[j:j + 1, :]
        u16 = u.astype(BF16)
        gelu, dgelu = _gelu_and_grad(yb_ref[...])
        dy = dy_ref[...]
        duy_ref[1] = (dy * (hf_ref[...] + hb_ref[...]) * dgelu).astype(BF16)
        dh_s[...] = dy * gelu
        a_s0[...] = _shift_rows(am_ref[0], -1)
        a_s1[...] = _shift_rows(am_ref[2], 1)
        _scans([(a_s0, dh_s, g_s0, True), (a_s1, dh_s, g_s1, False)])
        du = jnp.zeros((T, CG), F32)
        for d, g_s in enumerate((g_s0, g_s1)):
            reverse = d == 1
            wa = _pair_block_diag(wa_ref[d, 0], dup, same_half)
            wi = _pair_block_diag(wi_ref[d, 0], dup, same_half)
            lam_d = lam_ref[d:d + 1, :]
            r = _sigmoid(_dot(u16, wa) + ba_ref[d:d + 1, :])
            ig = _sigmoid(_dot(u16, wi) + bi_ref[d:d + 1, :])
            sp = _softplus(-lam_d)
            a, mult = am_ref[2 * d], am_ref[2 * d + 1]
            mult2 = mult * mult
            g = g_s[...]
            h_prev = _shift_rows(hb_ref[...], -1) if reverse else _shift_rows(hf_ref[...], 1)
            da = g * h_prev
            dmult = g * (ig * u)
            dig = g * mult * u
            du = du + g * mult * ig
            dmult_dlog = jnp.where(mult2 > 0.0, -(a * a) * lax.rsqrt(mult2), 0.0)
            dlog_a = da * a + dmult * dmult_dlog
            dr = dlog_a * ((-LRU_C) * sp)
            dsp = jnp.sum(dlog_a * ((-LRU_C) * r), axis=0, keepdims=True)
            dlam_ref[d:d + 1, :] = dsp * (-_sigmoid(-lam_d))
            dga = dr * r * (1.0 - r)
            dgi = dig * ig * (1.0 - ig)
            dga16 = dga.astype(BF16)
            dgi16 = dgi.astype(BF16)
            du = du + _dot_nt(dga16, wa) + _dot_nt(dgi16, wi)
            dwa_ref[d, 0] = _dot_exact(jnp.where(same_half, _dot_tn(u16, dga16), 0.0), dup_t)
            dwi_ref[d, 0] = _dot_exact(jnp.where(same_half, _dot_tn(u16, dgi16), 0.0), dup_t)
            dba_ref[d:d + 1, :] = jnp.sum(dga, axis=0, keepdims=True)
            dbi_ref[d:d + 1, :] = jnp.sum(dgi, axis=0, keepdims=True)
        dcb_ref[...] = jnp.sum(du, axis=0, keepdims=True)
        for j in range(4):
            dcw_ref[j:j + 1, :] = jnp.sum(du * taps[j], axis=0, keepdims=True)
        dup_in = (_shift_rows(du, -2) * cw_ref[0:1, :] + _shift_rows(du, -1) * cw_ref[1:2, :]
                  + du * cw_ref[2:3, :] + _shift_rows(du, 1) * cw_ref[3:4, :])
        duy_ref[0] = dup_in.astype(BF16)

    wshape = jax.ShapeDtypeStruct((2, N_CG, CG, REC_BLOCK), F32)
    vec = lambda rows: jax.ShapeDtypeStruct((rows, D_REC), F32)
    dup_np = _dup_table()
    return pl.pallas_call(
        body, name="rec_bwd",
        out_shape=(jax.ShapeDtypeStruct((2, T, D_REC), BF16),
                   vec(4), vec(1), wshape, vec(2), wshape, vec(2), vec(2)),
        grid=(N_CG,),
        in_specs=[tok(0), tok(N_CG), tok(0), tok(0), pl.BlockSpec((4, T, CG), lambda g: (0, 0, g)), tok(0),
                  per_ch(4), per_ch(1), wspec, per_ch(2), wspec, per_ch(2), per_ch(2),
                  const((REC_BLOCK, CG)), const((CG, REC_BLOCK)), const((CG, CG)),
                  pl.BlockSpec(memory_space=pl.ANY)],
        out_specs=(pl.BlockSpec((2, T, CG), lambda g: (0, 0, g)),
                   per_ch(4), per_ch(1), wspec, per_ch(2), wspec, per_ch(2), per_ch(2)),
        scratch_shapes=[pltpu.VMEM((T, CG), F32)] * 5,
        compiler_params=_params(dimension_semantics=("parallel",)),
    )(uy, uy, hf, hb, am, dyrec, conv_w, conv_b, w_a, b_a, w_i, b_i, lam,
      jnp.asarray(dup_np, BF16), jnp.asarray(dup_np.T.copy()), jnp.asarray(_pair_mask()), after)


TM_MIX = 256


def _mix_specs():
    tok = lambda width, blk=0: pl.BlockSpec((TM_MIX, width), lambda i: (i, blk))
    full = lambda shape: pl.BlockSpec(shape, lambda i: (0, 0))
    return tok, full


def _mix_fwd(x, att, yrec, gg, w_att_o_t, w_rec_o, w_out):
    tok, full = _mix_specs()

    def body(x_ref, att_ref, yr_ref, ga_ref, gr_ref, wao_ref, wro_ref, wo_ref, x1_ref, mixed_ref):
        y_att = _dot_nt(att_ref[...], wao_ref[...])
        y_rec = _dot(yr_ref[...], wro_ref[...])
        mixed = (_sigmoid(ga_ref[...]) * y_att + _sigmoid(gr_ref[...]) * y_rec).astype(BF16)
        mixed_ref[...] = mixed
        x1_ref[...] = x_ref[...] + _dot(mixed, wo_ref[...])

    return pl.pallas_call(
        body, name="mix_fwd",
        out_shape=(jax.ShapeDtypeStruct((T, D), F32), jax.ShapeDtypeStruct((T, D), BF16)),
        grid=(T // TM_MIX,),
        in_specs=[tok(D), tok(D_ATT), tok(D_REC), tok(D, 0), tok(D, 1),
                  full((D, D_ATT)), full((D_REC, D)), full((D, D))],
        out_specs=(tok(D), tok(D)),
        compiler_params=_params(dimension_semantics=("parallel",)),
    )(x, att, yrec, gg, gg, w_att_o_t, w_rec_o, w_out)


def _mix_bwd(dx1, att, yrec, gg, w_att_o_t, w_rec_o, w_out, after):
    tok, full = _mix_specs()

    def body(dx_ref, att_ref, yr_ref, ga_ref, gr_ref, wao_ref, wro_ref, wo_ref, after_ref,
             dgg_ref, dya_ref, dyr_ref, datt_ref, dyrp_ref):
        dmixed = _dot_nt(dx_ref[...].astype(BF16), wo_ref[...])
        y_att = _dot_nt(att_ref[...], wao_ref[...])
        y_rec = _dot(yr_ref[...], wro_ref[...])
        sa = _sigmoid(ga_ref[...])
        sr = _sigmoid(gr_ref[...])
        dgg_ref[0] = (dmixed * y_att * sa * (1.0 - sa)).astype(BF16)
        dgg_ref[1] = (dmixed * y_rec * sr * (1.0 - sr)).astype(BF16)
        dya = (dmixed * sa).astype(BF16)
        dyr = (dmixed * sr).astype(BF16)
        dya_ref[...] = dya
        dyr_ref[...] = dyr
        datt_ref[...] = _dot(dya, wao_ref[...]).astype(BF16)
        dyrp_ref[...] = _dot_nt(dyr, wro_ref[...])

    return pl.pallas_call(
        body, name="mix_bwd",
        out_shape=(jax.ShapeDtypeStruct((2, T, D), BF16),
                   jax.ShapeDtypeStruct((T, D), BF16), jax.ShapeDtypeStruct((T, D), BF16),
                   jax.ShapeDtypeStruct((T, D_ATT), BF16), jax.ShapeDtypeStruct((T, D_REC), F32)),
        grid=(T // TM_MIX,),
        in_specs=[tok(D), tok(D_ATT), tok(D_REC), tok(D, 0), tok(D, 1),
                  full((D, D_ATT)), full((D_REC, D)), full((D, D)), pl.BlockSpec(memory_space=pl.ANY)],
        out_specs=(pl.BlockSpec((2, TM_MIX, D), lambda i: (0, i, 0)),
                   tok(D), tok(D), tok(D_ATT), tok(D_REC)),
        compiler_params=_params(dimension_semantics=("parallel",)),
    )(dx1, att, yrec, gg, gg, w_att_o_t, w_rec_o, w_out, after)


TM_FFN = 256
FF_CHUNK = 1024


def _ffn_loss(x1, target, g2, gf, w_ff1_t, w_ff2):
    n_chunks = D_FF // FF_CHUNK

    def body(x1_ref, tg_ref, g2_ref, gf_ref, w1_hbm, w2_hbm,
             loss_ref, dx1_ref, h2_ref, act_ref, dpre_ref, dx2_ref, dg2_ref, dgf_ref,
             w1, w2, relu_s):
        i = pl.program_id(0)

        @pl.when(i == 0)
        def _():
            pltpu.sync_copy(w1_hbm, w1)
            pltpu.sync_copy(w2_hbm, w2)
            loss_ref[...] = jnp.zeros_like(loss_ref)
            dg2_ref[...] = jnp.zeros_like(dg2_ref)
            dgf_ref[...] = jnp.zeros_like(dgf_ref)

        x1v = x1_ref[...]
        r2 = lax.rsqrt(jnp.mean(x1v * x1v, axis=-1, keepdims=True) + EPS)
        xh2 = x1v * r2
        h2 = (xh2 * g2_ref[...]).astype(BF16)
        h2_ref[...] = h2
        x2 = x1v
        for c in range(n_chunks):
            ff = slice(c * FF_CHUNK, (c + 1) * FF_CHUNK)
            rl = jnp.maximum(_dot_nt(h2, w1[ff, :]), 0.0)
            relu_s[:, ff] = rl
            act = (rl * rl).astype(BF16)
            act_ref[:, ff] = act
            x2 = x2 + _dot(act, w2[ff, :])
        r3 = lax.rsqrt(jnp.mean(x2 * x2, axis=-1, keepdims=True) + EPS)
        xh3 = x2 * r3
        err = xh3 * gf_ref[...] - tg_ref[...]
        loss_ref[...] += 0.5 * jnp.sum(jnp.mean(err * err, axis=-1, keepdims=True))
        dy = err * (1.0 / D)
        dgf_ref[...] += jnp.sum(dy * xh3, axis=0, keepdims=True)
        dx2 = _rms_bwd(dy, xh3, r3, gf_ref[...])
        dx2_16 = dx2.astype(BF16)
        dx2_ref[...] = dx2_16
        dh2 = jnp.zeros((TM_FFN, D), F32)
        for c in range(n_chunks):
            ff = slice(c * FF_CHUNK, (c + 1) * FF_CHUNK)
            dpre = (_dot_nt(dx2_16, w2[ff, :]) * (2.0 * relu_s[:, ff])).astype(BF16)
            dpre_ref[:, ff] = dpre
            dh2 = dh2 + _dot(dpre, w1[ff, :])
        dg2_ref[...] += jnp.sum(dh2 * xh2, axis=0, keepdims=True)
        dx1_ref[...] = dx2 + _rms_bwd(dh2, xh2, r2, g2_ref[...])

    tok = lambda width: pl.BlockSpec((TM_FFN, width), lambda i: (i, 0))
    vec = pl.BlockSpec((1, D), lambda i: (0, 0))
    hbm = pl.BlockSpec(memory_space=pl.ANY)
    return pl.pallas_call(
        body, name="ffn_loss",
        out_shape=(jax.ShapeDtypeStruct((8, 128), F32), jax.ShapeDtypeStruct((T, D), F32),
                   jax.ShapeDtypeStruct((T, D), BF16), jax.ShapeDtypeStruct((T, D_FF), BF16),
                   jax.ShapeDtypeStruct((T, D_FF), BF16), jax.ShapeDtypeStruct((T, D), BF16),
                   jax.ShapeDtypeStruct((1, D), F32), jax.ShapeDtypeStruct((1, D), F32)),
        grid=(T // TM_FFN,),
        in_specs=[tok(D), tok(D), vec, vec, hbm, hbm],
        out_specs=(pl.BlockSpec((8, 128), lambda i: (0, 0)), tok(D), tok(D), tok(D_FF), tok(D_FF), tok(D),
                   vec, vec),
        scratch_shapes=[pltpu.VMEM((D_FF, D), BF16), pltpu.VMEM((D_FF, D), BF16),
                        pltpu.VMEM((TM_FFN, D_FF), F32)],
        compiler_params=_params(dimension_semantics=("arbitrary",)),
    )(x1, target, g2, gf, w_ff1_t, w_ff2)


def _local_step(x, target, p, late_weights, late_weights_ready, reduce_first, reduce_early, reduce_late):
    bias = _rpb_rows(p["rpb"])
    pairs = lambda w: w.reshape(2, N_CG, CG, REC_BLOCK)
    w_a, w_i = pairs(p["w_rg_a"]), pairs(p["w_rg_i"])
    rec_params = (p["conv_w"], p["conv_b"], w_a, p["b_rg_a"], w_i, p["b_rg_i"], p["lru_lambda"])

    qkv, uy, gg, h = _in_proj(x, p["ln1_g"], p["w_in_t"], p["b_in"], p["later_weights_started"])
    hf, hb, yrec, am = _rec_fwd(uy, *rec_params)
    att = _att_fwd(qkv, bias, late_weights(yrec, 0))
    p = {**p, **late_weights_ready(late_weights(att, 1), 0)}
    x1, mixed = _mix_fwd(x, att, yrec, gg, p["w_att_o_t"], p["w_rec_o"], p["w_out"])
    p = {**p, **late_weights_ready(x1, 1)}
    loss8, dx1, h2, act, dpre, dx2, g_ln2, g_lnf = _ffn_loss(
        x1, target, p["ln2_g"], p["lnf_g"], p["w_ff1_t"], p["w_ff2"])

    grads = {"ln2_g": g_ln2, "lnf_g": g_lnf,
             "w_ff1_t": _matmul(dpre, h2, "tn", BF16, "g_w_ff1"),
             "w_ff2": _matmul(act, dx2, "tn", BF16, "g_w_ff2")}
    dgg, dya, dyr, datt, dyrp = _mix_bwd(dx1, att, yrec, gg, p["w_att_o_t"], p["w_rec_o"], p["w_out"],
                                         reduce_first(grads, None))
    duy, g_cw, g_cb, g_wa, g_ba, g_wi, g_bi, g_lam = _rec_bwd(uy, hf, hb, am, dyrp, *rec_params,
                                                              reduce_first(None, dgg))
    blocks = lambda g: g.reshape(2, N_REC_BLOCKS, REC_BLOCK, REC_BLOCK)
    grads.update({
        "w_att_o_t": _matmul(dya, att, "tn", BF16, "g_w_att_o"),
        "conv_w": g_cw, "conv_b": g_cb, "w_rg_a": blocks(g_wa), "b_rg_a": g_ba,
        "w_rg_i": blocks(g_wi), "b_rg_i": g_bi, "lru_lambda": g_lam,
        "w_rec_o": _matmul(yrec, dyr, "tn", BF16, "g_w_rec_o"),
        "w_out": _matmul(mixed, dx1, "tn", BF16, "g_w_out"),
    })
    dqkv, gbias = _att_bwd(qkv, bias, datt, reduce_early(grads))
    dz = (dqkv, duy, dgg)
    g_w_in_t, g_b_in = _grad_w_in(dz, h)
    grads.update(w_in_t=g_w_in_t, b_in=g_b_in)
    grad_x, g_ln1 = _dh_norm1_bwd(dz, p["w_in_t"], x, p["ln1_g"], dx1, reduce_late(grads))
    grads.update(ln1_g=g_ln1, rpb=_rpb_fold(gbias))
    return loss8[0:1, 0:1], grad_x, grads


MESH_ID = pl.DeviceIdType.MESH
ANY = pl.BlockSpec(memory_space=pl.ANY)

CHAN_BLOCK_ROWS = 32
GATE_ROWS = 2 * 2 * N_REC_BLOCKS * REC_BLOCK * REC_BLOCK // (N_DEV * D)
SECTIONS = (("w_in_t", 704, D), ("w_rec_o", 128, D), ("w_out", 128, D), ("w_ff1_t", 512, D),
            ("w_ff2", 512, D), ("chan", CHAN_BLOCK_ROWS, D), ("w_att_o_t", 128, D_ATT),
            ("gates", GATE_ROWS, D))
N_SEC = len(SECTIONS)
N_CHAN_ROWS = 10
CHAN = (("conv_w", 4), ("b_rg_a", 2), ("b_rg_i", 2), ("lru_lambda", 2))


def _position():
    return lax.axis_index("x"), lax.axis_index("y"), lax.axis_index("c")


def _other_chips(x, y):
    return [(1 - x, y), (x, 1 - y), (1 - x, 1 - y)]


PASS_ON_IDS, PAIR_EARLY_ID, PAIR_LATE_ID, PAIR_FIRST_ID, SMALL_PASS_ON_ID = (1, 4), 2, 3, 5, 6


def _pair_handshake(x, y, c):
    barrier = pltpu.get_barrier_semaphore()
    pl.semaphore_signal(barrier, inc=1, device_id=(x, y, 1 - c), device_id_type=MESH_ID)
    pl.semaphore_wait(barrier, 1)


def _block_of(ref, dev, rows):
    return ref.at[pl.ds(pl.multiple_of(dev * rows, 16), rows)]


def _all_gather(shards, name):
    ns = len(shards)

    def body(*refs):
        x_refs, out_refs, done_ref = refs[:ns], refs[ns:2 * ns], refs[2 * ns]
        send_sems, recv_sems, local_sems = refs[2 * ns + 1:]
        done_ref[0, 0] = 0.0
        x, y, c = _position()
        me, sibling = (x, y, c), (x, y, 1 - c)
        x_nbr, y_nbr, diagonal = _other_chips(x, y)
        north = c == 1
        relay_from = (jnp.where(north, x_nbr[0], y_nbr[0]), jnp.where(north, x_nbr[1], y_nbr[1]))
        relay_to = (jnp.where(north, y_nbr[0], x_nbr[0]), jnp.where(north, y_nbr[1], x_nbr[1]))

        def rows(s, px, py, pc):
            return _block_of(out_refs[s], 4 * px + 2 * py + pc, shards[s].shape[0])

        def copy(k, s, block, to, from_shard=False):
            return pltpu.make_async_remote_copy(
                src_ref=x_refs[s] if from_shard else rows(s, *block), dst_ref=rows(s, *block),
                send_sem=send_sems.at[k * ns + s], recv_sem=recv_sems.at[k * ns + s],
                device_id=to, device_id_type=MESH_ID)

        sections = range(ns)
        mine = [pltpu.make_async_copy(x_refs[s], rows(s, *me), local_sems.at[s]) for s in sections]
        sent = [copy(k, s, me, to, True) for k, to in enumerate((sibling, (*x_nbr, c), (*y_nbr, c)))
                for s in sections]
        for cp in mine + sent:
            cp.start()
        for s in sections:
            copy(1, s, (*x_nbr, c), me).wait_recv()
            copy(2, s, (*y_nbr, c), me).wait_recv()
            sent += [copy(3, s, (*relay_from, c), (*relay_to, c)),
                     copy(4, s, (*x_nbr, c), sibling), copy(5, s, (*y_nbr, c), sibling)]
            for cp in sent[-3:]:
                cp.start()
        for s in sections:
            copy(3, s, (*diagonal, c), me).wait_recv()
            sent.append(copy(6, s, (*diagonal, c), sibling))
            sent[-1].start()
        for s in sections:
            copy(0, s, sibling, me).wait_recv()
            for k, chip in ((4, x_nbr), (5, y_nbr), (6, diagonal)):
                copy(k, s, (*chip, 1 - c), me).wait_recv()
        for cp in sent:
            cp.wait_send()
        for cp in mine:
            cp.wait()

    return pl.pallas_call(
        body, name=name,
        out_shape=tuple(jax.ShapeDtypeStruct((N_DEV * s.shape[0], s.shape[1]), s.dtype) for s in shards)
        + (jax.ShapeDtypeStruct((1, 1), F32),),
        in_specs=[ANY] * ns,
        out_specs=(ANY,) * ns + (pl.BlockSpec(memory_space=pltpu.SMEM),),
        scratch_shapes=[pltpu.SemaphoreType.DMA((7 * ns,)), pltpu.SemaphoreType.DMA((7 * ns,)),
                        pltpu.SemaphoreType.DMA((ns,))],
    )(*shards)


HBM = pl.BlockSpec(memory_space=pltpu.HBM)
SEM = pl.BlockSpec(memory_space=pltpu.SEMAPHORE)
EFFECT = pltpu.SideEffectType.DATAFLOW_SIDE_EFFECTING


def _in_hbm(a):
    return pltpu.with_memory_space_constraint(a, pltpu.HBM)


def _first_hop_copies(rows, which, x_refs, zones, send_sems, recv_sems):
    ns = len(rows)
    x, y, c = _position()
    targets = [(x, y, 1 - c)] + [(cx, cy, c) for cx, cy in _other_chips(x, y)]
    return [pltpu.make_async_remote_copy(
        src_ref=x_refs[i], dst_ref=_block_of(zones[i], 4 * x + 2 * y + c, rows[s]),
        send_sem=send_sems.at[k * ns + s], recv_sem=recv_sems.at[k * ns + s],
        device_id=to, device_id_type=MESH_ID)
        for k, to in enumerate(targets) for i, s in enumerate(which)]


def _after_all(arrays, name):
    def body(*refs):
        refs[-1][...] = jnp.zeros_like(refs[-1])

    return pl.pallas_call(
        body, name=name,
        out_shape=jax.ShapeDtypeStruct((8, LANES), F32),
        in_specs=[pl.BlockSpec(memory_space=pl.ANY)] * len(arrays),
        out_specs=pl.BlockSpec(memory_space=pltpu.VMEM),
    )(*arrays)


def _own_blocks_placed(shards, after, name):
    ns = len(shards)
    x, y, c = _position()
    me = jnp.reshape(4 * x + 2 * y + c, (1,)).astype(jnp.int32)
    if after is not None:
        shards = [*shards[:-1], shards[-1] + after.astype(shards[-1].dtype)]

    def body(me_ref, *refs):
        for s in range(ns):
            refs[ns + s][...] = refs[s][...]

    return pl.pallas_call(
        body, name=name,
        out_shape=tuple(jax.ShapeDtypeStruct((N_DEV * s.shape[0], s.shape[1]), s.dtype) for s in shards),
        grid_spec=pltpu.PrefetchScalarGridSpec(
            num_scalar_prefetch=1, grid=(1,),
            in_specs=[pl.BlockSpec(s.shape, lambda i, me: (0, 0)) for s in shards],
            out_specs=tuple(pl.BlockSpec(s.shape, lambda i, me: (me[0], 0)) for s in shards)),
        compiler_params=_params(dimension_semantics=("arbitrary",)),
    )(me, *shards)


def _gather_start(shards, after, name):
    ns = len(shards)
    zones = _own_blocks_placed(shards, after, name + "_own_blocks")

    def body(*refs):
        for cp in _first_hop_copies([s.shape[0] for s in shards], range(ns), refs[:ns], refs[ns:2 * ns],
                                    refs[2 * ns], refs[2 * ns + 1]):
            cp.start()
        refs[-1][...] = jnp.zeros_like(refs[-1])

    out = pl.pallas_call(
        body, name=name,
        out_shape=(pltpu.SemaphoreType.DMA((4 * ns,)), pltpu.SemaphoreType.DMA((4 * ns,)),
                   *[pltpu.HBM(a.shape, a.dtype) for a in (*shards, *zones)],
                   jax.ShapeDtypeStruct((8, LANES), F32)),
        in_specs=[HBM] * (2 * ns),
        out_specs=(SEM, SEM, *[HBM] * (2 * ns), pl.BlockSpec(memory_space=pltpu.VMEM)),
        input_output_aliases={i: 2 + i for i in range(2 * ns)},
        compiler_params=pltpu.CompilerParams(has_side_effects=EFFECT),
    )(*[_in_hbm(a) for a in shards], *[_in_hbm(a) for a in zones])
    return out[0], out[1], out[2:2 + ns], out[2 + ns:2 + 2 * ns], out[-1]


def _gather_wait(send_sems, recv_sems, rows, which, shards, zones, after, name):
    ns = len(shards)

    def body(*refs):
        for cp in _first_hop_copies(rows, which, refs[:ns], refs[ns:2 * ns], refs[2 * ns], refs[2 * ns + 1]):
            cp.wait_send()
            cp.wait_recv()

    out = pl.pallas_call(
        body, name=name,
        out_shape=tuple(pltpu.HBM(a.shape, a.dtype) for a in (*shards, *zones)),
        in_specs=[HBM] * (2 * ns) + [SEM, SEM, ANY],
        out_specs=(HBM,) * (2 * ns),
        input_output_aliases={i: i for i in range(2 * ns)},
        compiler_params=pltpu.CompilerParams(has_side_effects=EFFECT),
    )(*shards, *zones, send_sems, recv_sems, after)
    return out[:ns], out[ns:]


def _pass_on_copies(rows, in_refs, out_refs, send_sems, recv_sems):
    ns = len(rows)
    x, y, c = _position()
    return [pltpu.make_async_remote_copy(
        src_ref=_block_of(in_refs[s], 4 * cx + 2 * cy + c, rows[s]),
        dst_ref=_block_of(out_refs[s], 4 * cx + 2 * cy + c, rows[s]),
        send_sem=send_sems.at[j * ns + s], recv_sem=recv_sems.at[j * ns + s],
        device_id=(x, y, 1 - c), device_id_type=MESH_ID)
        for j, (cx, cy) in enumerate(_other_chips(x, y)) for s in range(ns)]


def _pass_on_start(rows, zones, barrier_id, name):
    ns = len(zones)

    def body(*refs):
        _pair_handshake(*_position())
        for cp in _pass_on_copies(rows, refs[:ns], refs[:ns], refs[ns], refs[ns + 1]):
            cp.start()
        refs[-1][...] = jnp.zeros_like(refs[-1])

    out = pl.pallas_call(
        body, name=name,
        out_shape=(pltpu.SemaphoreType.DMA((3 * ns,)), pltpu.SemaphoreType.DMA((3 * ns,)),
                   *[pltpu.HBM(z.shape, z.dtype) for z in zones], jax.ShapeDtypeStruct((8, LANES), F32)),
        in_specs=[HBM] * ns,
        out_specs=(SEM, SEM, *[HBM] * ns, pl.BlockSpec(memory_space=pltpu.VMEM)),
        input_output_aliases={i: 2 + i for i in range(ns)},
        compiler_params=pltpu.CompilerParams(has_side_effects=EFFECT, collective_id=barrier_id),
    )(*[_in_hbm(z) for z in zones])
    return out[0], out[1], out[2:2 + ns], out[-1]


def _pass_on_wait(rows, send_sems, recv_sems, zones, after, name):
    ns = len(zones)

    def body(*refs):
        for cp in _pass_on_copies(rows, refs[:ns], refs[:ns], refs[ns], refs[ns + 1]):
            cp.wait_send()
            cp.wait_recv()

    return pl.pallas_call(
        body, name=name,
        out_shape=tuple(pltpu.HBM(z.shape, z.dtype) for z in zones),
        in_specs=[HBM] * ns + [SEM, SEM, ANY],
        out_specs=(HBM,) * ns,
        input_output_aliases={i: i for i in range(ns)},
        compiler_params=pltpu.CompilerParams(has_side_effects=EFFECT),
    )(*zones, send_sems, recv_sems, after)


def _gather_pass_on(rows, zones, barrier_id, name):
    ns = len(zones)

    def body(*refs):
        _pair_handshake(*_position())
        copies = _pass_on_copies(rows, refs[:ns], refs[ns:2 * ns], *refs[2 * ns:])
        for cp in copies:
            cp.start()
        for cp in copies:
            cp.wait_recv()
        for cp in copies:
            cp.wait_send()

    return pl.pallas_call(
        body, name=name,
        out_shape=tuple(jax.ShapeDtypeStruct(z.shape, z.dtype) for z in zones),
        in_specs=[ANY] * ns, out_specs=(ANY,) * ns,
        input_output_aliases={i: i for i in range(ns)},
        scratch_shapes=[pltpu.SemaphoreType.DMA((3 * ns,)), pltpu.SemaphoreType.DMA((3 * ns,))],
        compiler_params=pltpu.CompilerParams(collective_id=barrier_id),
    )(*zones)


def _pair_copies(sections, g_refs, land, send_sems, recv_sems):
    ns = len(sections)
    x, y, c = _position()
    return [pltpu.make_async_remote_copy(
        src_ref=_block_of(g_refs[s], 2 * k + 1 - c, rows), dst_ref=land[s].at[k],
        send_sem=send_sems.at[k * ns + s], recv_sem=recv_sems.at[k * ns + s],
        device_id=(x, y, 1 - c), device_id_type=MESH_ID)
        for k in range(N_CHIPS) for s, (_, rows, _) in enumerate(sections)]


def _pair_exchange_start(sections, grads, barrier_id, name):
    ns = len(sections)

    def body(*refs):
        _pair_handshake(*_position())
        for cp in _pair_copies(sections, refs[:ns], refs[ns:2 * ns], refs[2 * ns], refs[2 * ns + 1]):
            cp.start()
        refs[-1][...] = jnp.zeros_like(refs[-1])

    zones = [lax.empty((N_CHIPS, rows, cols), BF16) for _, rows, cols in sections]
    n = N_CHIPS * ns
    out = pl.pallas_call(
        body, name=name,
        out_shape=(pltpu.SemaphoreType.DMA((n,)), pltpu.SemaphoreType.DMA((n,)),
                   *[pltpu.HBM(a.shape, a.dtype) for a in (*grads, *zones)],
                   jax.ShapeDtypeStruct((8, LANES), F32)),
        in_specs=[HBM] * (2 * ns),
        out_specs=(SEM, SEM, *[HBM] * (2 * ns), pl.BlockSpec(memory_space=pltpu.VMEM)),
        input_output_aliases={i: 2 + i for i in range(2 * ns)},
        compiler_params=pltpu.CompilerParams(has_side_effects=EFFECT, collective_id=barrier_id),
    )(*[_in_hbm(a) for a in grads], *[_in_hbm(a) for a in zones])
    return out[0], out[1], out[2:2 + ns], out[2 + ns:2 + 2 * ns], out[-1]


def _pair_exchange_wait(sections, send_sems, recv_sems, grads, zones, after, name):
    ns = len(sections)

    def body(*refs):
        for cp in _pair_copies(sections, refs[:ns], refs[ns:2 * ns], refs[2 * ns], refs[2 * ns + 1]):
            cp.wait_send()
            cp.wait_recv()

    out = pl.pallas_call(
        body, name=name,
        out_shape=tuple(pltpu.HBM(a.shape, a.dtype) for a in (*grads, *zones)),
        in_specs=[HBM] * (2 * ns) + [SEM, SEM, ANY],
        out_specs=(HBM,) * (2 * ns),
        input_output_aliases={i: i for i in range(2 * ns)},
        compiler_params=pltpu.CompilerParams(has_side_effects=EFFECT),
    )(*grads, *zones, send_sems, recv_sems, after)
    return out[:ns], out[ns:]


def _pair_add(sections, grads, got, core, name):
    ns = len(sections)

    def body(core_ref, *refs):
        g_refs, got_refs, p_refs = refs[:ns], refs[ns:2 * ns], refs[2 * ns:]
        for s in range(ns):
            p_refs[s][0] = (g_refs[s][...].astype(F32) + got_refs[s][0].astype(F32)).astype(BF16)

    slot = [pl.BlockSpec((1, rows, cols), lambda k, c: (k, 0, 0)) for _, rows, cols in sections]
    return pl.pallas_call(
        body, name=name,
        out_shape=tuple(jax.ShapeDtypeStruct((N_CHIPS, rows, cols), BF16) for _, rows, cols in sections),
        grid_spec=pltpu.PrefetchScalarGridSpec(
            num_scalar_prefetch=1, grid=(N_CHIPS,),
            in_specs=[pl.BlockSpec((rows, cols), lambda k, c: (2 * k + c[0], 0)) for _, rows, cols in sections]
            + slot,
            out_specs=tuple(slot)),
        compiler_params=_params(dimension_semantics=("parallel",)),
    )(core, *grads, *got)


def _chip_copies(sections, p_refs, land, send_sems, recv_sems):
    ns = len(sections)
    x, y, c = _position()
    return [pltpu.make_async_remote_copy(
        src_ref=p_refs[s].at[2 * cx + cy], dst_ref=land[s].at[j],
        send_sem=send_sems.at[j * ns + s], recv_sem=recv_sems.at[j * ns + s],
        device_id=(cx, cy, c), device_id_type=MESH_ID)
        for j, (cx, cy) in enumerate(_other_chips(x, y)) for s in range(ns)]


def _chip_exchange(sections, parts, name):
    ns = len(sections)

    def body(*refs):
        copies = _chip_copies(sections, refs[:ns], refs[ns:2 * ns], *refs[2 * ns:])
        for cp in copies:
            cp.start()
        for cp in copies:
            cp.wait_recv()
        for cp in copies:
            cp.wait_send()

    n = 3 * ns
    return pl.pallas_call(
        body, name=name,
        out_shape=tuple(jax.ShapeDtypeStruct((3, rows, cols), BF16) for _, rows, cols in sections),
        in_specs=[ANY] * ns, out_specs=(ANY,) * ns,
        scratch_shapes=[pltpu.SemaphoreType.DMA((n,)), pltpu.SemaphoreType.DMA((n,))],
    )(*parts)


def _chip_exchange_start(sections, parts, name):
    ns = len(sections)

    def body(*refs):
        p_refs, land = refs[:ns], refs[ns:2 * ns]
        send_sems, recv_sems = refs[2 * ns], refs[2 * ns + 1]
        token = refs[-1]
        for cp in _chip_copies(sections, p_refs, land, send_sems, recv_sems):
            cp.start()
        token[...] = jnp.zeros_like(token)

    zones = [lax.empty((3, rows, cols), BF16) for _, rows, cols in sections]
    out = pl.pallas_call(
        body, name=name,
        out_shape=(pltpu.SemaphoreType.DMA((3 * ns,)), pltpu.SemaphoreType.DMA((3 * ns,)),
                   *[pltpu.HBM(a.shape, a.dtype) for a in parts], *[pltpu.HBM(a.shape, a.dtype) for a in zones],
                   jax.ShapeDtypeStruct((8, LANES), F32)),
        in_specs=[HBM] * (2 * ns),
        out_specs=(SEM, SEM, *[HBM] * (2 * ns), pl.BlockSpec(memory_space=pltpu.VMEM)),
        input_output_aliases={i: 2 + i for i in range(2 * ns)},
        compiler_params=pltpu.CompilerParams(has_side_effects=EFFECT),
    )(*[_in_hbm(a) for a in parts], *[_in_hbm(a) for a in zones])
    return out[0], out[1], out[2:2 + ns], out[2 + ns:2 + 2 * ns], out[-1]


def _chip_exchange_wait(sections, send_sems, recv_sems, parts, zones, after, name):
    ns = len(sections)

    def body(*refs):
        p_refs, land = refs[:ns], refs[ns:2 * ns]
        for cp in _chip_copies(sections, p_refs, land, refs[2 * ns], refs[2 * ns + 1]):
            cp.wait_send()
            cp.wait_recv()

    out = pl.pallas_call(
        body, name=name,
        out_shape=tuple(pltpu.HBM(a.shape, a.dtype) for a in (*parts, *zones)),
        in_specs=[HBM] * (2 * ns) + [SEM, SEM, ANY],
        out_specs=(HBM,) * (2 * ns),
        input_output_aliases={i: i for i in range(2 * ns)},
        compiler_params=pltpu.CompilerParams(has_side_effects=EFFECT),
    )(*parts, *zones, send_sems, recv_sems, after)
    return out[:ns], out[ns:]


def _grad_finish(sections, parts, far, chip, name):
    ns = len(sections)

    def body(chip_ref, *refs):
        p_refs, b_refs, g_refs = refs[:ns], refs[ns:2 * ns], refs[2 * ns:]
        for s in range(ns):
            g = p_refs[s][0].astype(F32)
            for j in range(3):
                g = g + b_refs[s][j].astype(F32)
            g_refs[s][...] = g

    half = [(rows // 2, cols) for _, rows, cols in sections]
    return pl.pallas_call(
        body, name=name,
        out_shape=tuple(jax.ShapeDtypeStruct((rows, cols), F32) for _, rows, cols in sections),
        grid_spec=pltpu.PrefetchScalarGridSpec(
            num_scalar_prefetch=1, grid=(2,),
            in_specs=[pl.BlockSpec((1, r, c), lambda i, chip: (chip[0], i, 0)) for r, c in half]
            + [pl.BlockSpec((3, r, c), lambda i, chip: (0, i, 0)) for r, c in half],
            out_specs=tuple(pl.BlockSpec((r, c), lambda i, chip: (i, 0)) for r, c in half)),
        compiler_params=_params(dimension_semantics=("parallel",)),
    )(chip, *parts, *far)


def _sum_devices(parts, rows, name):
    cols = parts.shape[1]
    tr = rows // 2

    def body(*refs):
        s = refs[0][...].astype(F32)
        for d in range(1, N_DEV):
            s = s + refs[d][...].astype(F32)
        refs[N_DEV][...] = s

    return pl.pallas_call(
        body, name=name,
        out_shape=jax.ShapeDtypeStruct((rows, cols), F32),
        grid=(2,),
        in_specs=[pl.BlockSpec((tr, cols), lambda i, d=d: (2 * d + i, 0)) for d in range(N_DEV)],
        out_specs=pl.BlockSpec((tr, cols), lambda i: (i, 0)),
        compiler_params=_params(dimension_semantics=("parallel",)),
    )(*([parts] * N_DEV))


def _adamw_step(w_ref, g_ref, m_ref, v_ref, d_ref, nm_ref, nv_ref):
    c1 = 1.0 / (1.0 - ADAM_B1 ** ADAM_STEP)
    c2 = 1.0 / (1.0 - ADAM_B2 ** ADAM_STEP)
    gv = g_ref[...]
    nm = ADAM_B1 * m_ref[...] + (1.0 - ADAM_B1) * gv
    nv = ADAM_B2 * v_ref[...] + (1.0 - ADAM_B2) * (gv * gv)
    nm_ref[...] = nm
    nv_ref[...] = nv
    d_ref[...] = (-ADAM_LR) * ((nm * c1) / (jnp.sqrt(nv * c2) + ADAM_EPS) + ADAM_WD * w_ref[...])


def _adamw_small(params, name):
    n = len(params)

    def body(*refs):
        for k in range(n):
            _adamw_step(*refs[4 * k:4 * k + 4], *refs[4 * n + 3 * k:4 * n + 3 * k + 3])

    out = pl.pallas_call(
        body, name=name,
        out_shape=tuple(jax.ShapeDtypeStruct(p[0].shape, F32) for p in params for _ in range(3)),
    )(*[a for p in params for a in p])
    return [out[3 * k:3 * k + 3] for k in range(n)]


def _adamw(w, g, m, v, name, after=None):
    rows, cols = w.shape
    tr = rows
    while tr * cols * 4 > (1 << 20) and tr % 16 == 0:
        tr //= 2
    tokens = [] if after is None else [after]

    def body(*refs):
        _adamw_step(*refs[:4], *refs[4 + len(tokens):])

    spec = pl.BlockSpec((tr, cols), lambda i: (i, 0))
    shape = jax.ShapeDtypeStruct((rows, cols), F32)
    return pl.pallas_call(
        body, name=name,
        out_shape=(shape, shape, shape),
        grid=(rows // tr,),
        in_specs=[spec] * 4 + [ANY] * len(tokens), out_specs=(spec,) * 3,
        compiler_params=_params(dimension_semantics=("parallel",)),
    )(w, g, m, v, *tokens)


NAMES = ("ln1_g", "w_in", "b_in", "rpb", "w_att_o", "conv_w", "conv_b", "w_rg_a", "b_rg_a", "w_rg_i",
         "b_rg_i", "lru_lambda", "w_rec_o", "w_out", "ln2_g", "w_ff1", "w_ff2", "lnf_g")
TRANSPOSED = {"w_in": "w_in_t", "w_att_o": "w_att_o_t", "w_ff1": "w_ff1_t"}
ROW_SHARDED = ("w_rec_o", "w_out", "w_ff2")
REPLICATED = (("ln1_g", (1, D)), ("b_in", (1, D_IN)), ("rpb", (N_HEADS * N_RPB_R, N_RPB_C)),
              ("conv_b", (1, D_REC)), ("w_rg_a", (2 * N_REC_BLOCKS * REC_BLOCK, REC_BLOCK)),
              ("w_rg_i", (2 * N_REC_BLOCKS * REC_BLOCK, REC_BLOCK)), ("ln2_g", (1, D)), ("lnf_g", (1, D)))
GATE_BLOCKS = ("w_rg_a", "w_rg_i")
SMALL_ROWS = 112


def _chan_bits(vectors):
    chan = jnp.concatenate(vectors, axis=0)
    bits = lax.bitcast_convert_type(chan, BF16).reshape(-1)
    return jnp.pad(bits, (0, CHAN_BLOCK_ROWS * D - bits.shape[0])).reshape(CHAN_BLOCK_ROWS, D)


def _chan_from_bits(gathered):
    bits = gathered.reshape(N_DEV, CHAN_BLOCK_ROWS * D)[:, :2 * N_CHAN_ROWS * LANES]
    chan = lax.bitcast_convert_type(bits.reshape(N_DEV, N_CHAN_ROWS, LANES, 2), F32)
    return chan.transpose(1, 0, 2).reshape(N_CHAN_ROWS, D)


def kernel(x, ln1_g, w_in, b_in, rpb, w_att_o, conv_w, conv_b, w_rg_a, b_rg_a, w_rg_i, b_rg_i, lru_lambda, w_rec_o, w_out, ln2_g, w_ff1, w_ff2, lnf_g, loss_target, m_ln1_g, m_w_in, m_b_in, m_rpb, m_w_att_o, m_conv_w, m_conv_b, m_w_rg_a, m_b_rg_a, m_w_rg_i, m_b_rg_i, m_lru_lambda, m_w_rec_o, m_w_out, m_ln2_g, m_w_ff1, m_w_ff2, m_lnf_g, v_ln1_g, v_w_in, v_b_in, v_rpb, v_w_att_o, v_conv_w, v_conv_b, v_w_rg_a, v_b_rg_a, v_w_rg_i, v_b_rg_i, v_lru_lambda, v_w_rec_o, v_w_out, v_ln2_g, v_w_ff1, v_w_ff2, v_lnf_g):
    w = dict(zip(NAMES, (ln1_g, w_in, b_in, rpb, w_att_o, conv_w, conv_b, w_rg_a, b_rg_a, w_rg_i,
                         b_rg_i, lru_lambda, w_rec_o, w_out, ln2_g, w_ff1, w_ff2, lnf_g)))
    m = dict(zip(NAMES, (m_ln1_g, m_w_in, m_b_in, m_rpb, m_w_att_o, m_conv_w, m_conv_b, m_w_rg_a,
                         m_b_rg_a, m_w_rg_i, m_b_rg_i, m_lru_lambda, m_w_rec_o, m_w_out, m_ln2_g,
                         m_w_ff1, m_w_ff2, m_lnf_g)))
    v = dict(zip(NAMES, (v_ln1_g, v_w_in, v_b_in, v_rpb, v_w_att_o, v_conv_w, v_conv_b, v_w_rg_a,
                         v_b_rg_a, v_w_rg_i, v_b_rg_i, v_lru_lambda, v_w_rec_o, v_w_out, v_ln2_g,
                         v_w_ff1, v_w_ff2, v_lnf_g)))
    xi, yi, ci = _position()

    shard = {t: w[n][0].T.astype(BF16) for n, t in TRANSPOSED.items()}
    shard.update({n: w[n][0].astype(BF16) for n in ROW_SHARDED})
    shard["chan"] = _chan_bits([w[n][0] for n, _ in CHAN])
    first, later = ("w_in_t", "chan"), ("w_rec_o", "w_out", "w_att_o_t", "w_ff1_t", "w_ff2")
    *gathered, done = _all_gather([shard[n] for n in first], "weight_all_gather")
    p = dict(zip(first, gathered))
    send_sems, recv_sems, sent, zones, token = _gather_start([shard[n] for n in later], done,
                                                             "weight_gather_start")

    stages = (("w_rec_o", "w_out", "w_att_o_t"), ("w_ff1_t", "w_ff2"))
    passing = {}

    def late_weights(after, stage):
        which = [later.index(n) for n in stages[stage]]
        _, arrived = _gather_wait(
            send_sems, recv_sems, [shard[n].shape[0] for n in later], which, [sent[i] for i in which],
            [zones[i] for i in which], after, "weight_gather_wait_%d" % stage)
        passing[stage] = _pass_on_start(
            [shard[n].shape[0] for n in stages[stage]], arrived,
            PASS_ON_IDS[stage], "weight_pass_on_start_%d" % stage)
        return passing[stage][-1]

    def late_weights_ready(after, stage):
        pass_send_sems, pass_recv_sems, pass_zones, _ = passing[stage]
        return dict(zip(stages[stage], _pass_on_wait(
            [shard[n].shape[0] for n in stages[stage]], pass_send_sems, pass_recv_sems, pass_zones, after,
            "weight_pass_on_wait_%d" % stage)))

    chan = _chan_from_bits(p.pop("chan"))
    r0 = 0
    for n, rows in CHAN:
        p[n] = chan[r0:r0 + rows]
        r0 += rows
    p.update(ln1_g=w["ln1_g"], b_in=w["b_in"], later_weights_started=token, rpb=w["rpb"][0], conv_b=w["conv_b"],
             w_rg_a=w["w_rg_a"][0], w_rg_i=w["w_rg_i"][0], ln2_g=w["ln2_g"],
             lnf_g=w["lnf_g"].reshape(1, D))

    core = jnp.reshape(ci, (1,)).astype(jnp.int32)
    chip = jnp.reshape(2 * xi + yi, (1,)).astype(jnp.int32)
    first_sections = tuple(s for s in SECTIONS if s[0] in ("w_ff1_t", "w_ff2"))
    late_sections = SECTIONS[:1]
    early_sections = tuple(s for s in SECTIONS[1:] if s not in first_sections)
    in_flight = {}

    def pair_sum_and_send(group, sections, after):
        send_sems, recv_sems, sect, zones, _ = in_flight["pair_" + group]
        sect, got = _pair_exchange_wait(sections, send_sems, recv_sems, sect, zones, after,
                                        "grad_pair_exchange_wait_" + group)
        parts = _pair_add(sections, sect, got, core, "grad_pair_add_" + group)
        in_flight[group] = _chip_exchange_start(sections, parts, "grad_chip_exchange_start_" + group)
        return in_flight[group][-1]

    def pair_exchange_at_once(group, sections, grads, barrier_id):
        in_flight["pair_" + group] = _pair_exchange_start(
            sections, [grads[n] for n, _, _ in sections], barrier_id, "grad_pair_exchange_start_" + group)
        return pair_sum_and_send(group, sections, in_flight["pair_" + group][-1])

    def reduce_first(grads, after):
        if grads is None:
            return pair_sum_and_send("first", first_sections, after)
        in_flight["pair_first"] = _pair_exchange_start(
            first_sections, [grads[n] for n, _, _ in first_sections], PAIR_FIRST_ID,
            "grad_pair_exchange_start_first")
        return in_flight["pair_first"][-1]

    def reduce_early(grads):
        chan_g = jnp.concatenate([grads[n] for n, _ in CHAN], axis=0)
        chan_g = chan_g.reshape(N_CHAN_ROWS, N_DEV, LANES).transpose(1, 0, 2).astype(BF16)
        chan_g = jnp.pad(chan_g.reshape(N_DEV, -1), ((0, 0), (0, CHAN_BLOCK_ROWS * D - N_CHAN_ROWS * LANES)))
        grads["chan"] = chan_g.reshape(N_DEV * CHAN_BLOCK_ROWS, D)
        grads["gates"] = jnp.concatenate([grads[n].reshape(-1, D) for n in GATE_BLOCKS], axis=0).astype(BF16)
        return pair_exchange_at_once("early", early_sections, grads, PAIR_EARLY_ID)

    def finish(group, sections, after, name):
        send_sems, recv_sems, parts, zones, _ = in_flight[group]
        parts, far = _chip_exchange_wait(sections, send_sems, recv_sems, parts, zones, after,
                                         "grad_chip_exchange_wait_" + name)
        return dict(zip((n for n, _, _ in sections),
                        _grad_finish(sections, parts, far, chip, "grad_finish_" + name)))

    summed = {}

    def reduce_late(grads):
        in_flight["pair_late"] = _pair_exchange_start(
            late_sections, [grads[n] for n, _, _ in late_sections], PAIR_LATE_ID,
            "grad_pair_exchange_start_late")
        summed.update(finish("first", first_sections, in_flight["pair_late"][-1], "first"))
        summed.update(finish("early", early_sections, summed["w_ff2"], "early"))
        return pair_sum_and_send("late", late_sections, summed["gates"])

    loss_part, grad_x, grads = _local_step(x[0], loss_target[0], p, late_weights, late_weights_ready,
                                           reduce_first, reduce_early, reduce_late)

    flat = jnp.concatenate([grads[n].reshape(-1) for n, _ in REPLICATED if n not in GATE_BLOCKS]
                           + [loss_part.reshape(-1)])
    n_small = flat.shape[0]
    flat = jnp.pad(flat, (0, SMALL_ROWS * LANES - n_small)).reshape(SMALL_ROWS, LANES)
    *small_gather, small_started = _gather_start([flat, summed["gates"]], None, "small_grad_gather_start")

    g, delta, new_m, new_v = {}, {}, {}, {}

    def update(n, g2, shape2, after=None):
        d2, m2, v2 = _adamw(w[n].reshape(shape2), g2, m[n].reshape(shape2), v[n].reshape(shape2),
                            "adamw_" + n, after)
        g[n], delta[n], new_m[n], new_v[n] = (a.reshape(w[n].shape) for a in (g2, d2, m2, v2))

    for n in ROW_SHARDED:
        update(n, summed[n], summed[n].shape, small_started)
    for n, t in TRANSPOSED.items():
        if t in summed:
            update(n, summed[t].T, summed[t].shape[::-1], small_started)

    small_rows = [SMALL_ROWS, GATE_ROWS]
    _, small_zones = _gather_wait(
        *small_gather[:2], small_rows, range(2), *small_gather[2:],
        _after_all(list(delta.values()), "sharded_updates_done"), "small_grad_gather_wait")
    small_parts, gate_sum = _gather_pass_on(small_rows, small_zones, SMALL_PASS_ON_ID,
                                            "small_grad_gather_pass_on")
    small = _sum_devices(small_parts, SMALL_ROWS, "small_grad_sum").reshape(-1)
    loss = small[n_small - 1]

    small_params = []
    o = 0
    for n, shape2 in REPLICATED:
        if n in GATE_BLOCKS:
            k, rows = GATE_BLOCKS.index(n), gate_sum.shape[0] // len(GATE_BLOCKS)
            update(n, gate_sum[k * rows:(k + 1) * rows].reshape(shape2), shape2)
        else:
            size = shape2[0] * shape2[1]
            small_params.append((n, small[o:o + size].reshape(shape2), shape2))
            o += size
    chan_back = summed["chan"].reshape(-1)[:N_CHAN_ROWS * LANES].reshape(N_CHAN_ROWS, LANES)
    r0 = 0
    for n, rows in CHAN:
        small_params.append((n, chan_back[r0:r0 + rows], (rows, LANES)))
        r0 += rows
    results = _adamw_small([(w[n].reshape(s2), g2, m[n].reshape(s2), v[n].reshape(s2))
                            for n, g2, s2 in small_params], "adamw_vectors")
    for (n, g2, _), (d2, m2, v2) in zip(small_params, results):
        g[n], delta[n], new_m[n], new_v[n] = (a.reshape(w[n].shape) for a in (g2, d2, m2, v2))

    summed = finish("late", late_sections, _after_all(list(delta.values()), "updates_done"), "late")
    g_t = summed["w_in_t"]
    results = _adamw(w["w_in"][0].T, g_t, m["w_in"][0].T, v["w_in"][0].T, "adamw_w_in")
    g["w_in"], delta["w_in"], new_m["w_in"], new_v["w_in"] = (a.T[None] for a in (g_t, *results))

    return (loss, grad_x[None], *[g[n] for n in NAMES], *[delta[n] for n in NAMES],
            *[new_m[n] for n in NAMES], *[new_v[n] for n in NAMES])
```

```python
import math

import numpy as np
import jax
import jax.numpy as jnp
from jax import lax
from jax.experimental import pallas as pl
from jax.experimental.pallas import tpu as pltpu

F32 = jnp.float32
BF16 = jnp.bfloat16

T = 2048
D = 1024
D_ATT = 512
D_REC = 1024
D_FF = 4096
D_IN = 5632
N_HEADS = 8
DH = 64
GRID_W = 64
ROWS = T // GRID_W
WIN_H = 8
WIN_W = 16
KWIN = WIN_H * GRID_W
N_RPB_R = 2 * WIN_H - 1
N_RPB_C = 2 * WIN_W - 1
N_REC_BLOCKS = 16
REC_BLOCK = 64
CG = 128
N_CG = D_REC // CG
LRU_C = 8.0
EPS = 1e-6
N_DEV = 8
N_CHIPS = 4
LANES = 128

ADAM_LR = 0.001
ADAM_B1 = 0.9
ADAM_B2 = 0.999
ADAM_EPS = 1e-08
ADAM_WD = 0.01
ADAM_STEP = 10

MESH_AXES = ("x", "y", "c")
VMEM_LIMIT = 56 * 1024 * 1024

TILE = 512
DZ_ARRAYS = ((0, 3, 1), (3, 4, 2), (7, 4, 2))
N_DZ_TILES = D_IN // TILE


def _params(**kw):
    return pltpu.CompilerParams(vmem_limit_bytes=VMEM_LIMIT, **kw)


HG = 4
HQ = HG * GRID_W
HC = HG * DH


def _att_tables():
    rq = np.arange(GRID_W)
    kc = np.arange(KWIN) % GRID_W
    win_start = np.clip(rq - WIN_W // 2, 0, GRID_W - WIN_W)
    valid = (kc[None, :] >= win_start[:, None]) & (kc[None, :] < win_start[:, None] + WIN_W)
    same_head = (np.arange(HQ)[:, None] // GRID_W) == (np.arange(HC)[None, :] // DH)
    return valid.astype(np.float32), same_head.astype(np.float32)


def _pair_mask():
    half = np.arange(2 * DH) // DH
    return (half[:, None] == half[None, :]).astype(np.float32)


def _dup_table():
    return np.concatenate([np.eye(REC_BLOCK, dtype=np.float32)] * 2, axis=1)


def _sigmoid(x):
    return 0.5 * jnp.tanh(0.5 * x) + 0.5


def _softplus(x):
    return jnp.maximum(x, 0.0) + jnp.log(1.0 + jnp.exp(-jnp.abs(x)))


def _one_minus_square(log_a, a):
    x = 2.0 * log_a
    series = -x * (1.0 + x * (0.5 + x * (1.0 / 6.0)))
    return jnp.where(x > -0.02, series, 1.0 - a * a)


_GELU_C = math.sqrt(2.0 / math.pi)


def _gelu_and_grad(x):
    x2 = x * x
    inner = _GELU_C * (x + 0.044715 * x * x2)
    t = jnp.tanh(inner)
    g = 0.5 * x * (1.0 + t)
    dg = 0.5 * (1.0 + t) + 0.5 * x * (1.0 - t * t) * _GELU_C * (1.0 + 3.0 * 0.044715 * x2)
    return g, dg


def _dot(a, b):
    return jnp.dot(a, b, preferred_element_type=F32)


def _dot_nt(a, b):
    return lax.dot_general(a, b, (((1,), (1,)), ((), ())), preferred_element_type=F32)


def _dot_tn(a, b):
    return lax.dot_general(a, b, (((0,), (0,)), ((), ())), preferred_element_type=F32)


def _dot_exact(a, b):
    return jnp.dot(a, b, precision=lax.Precision.HIGHEST, preferred_element_type=F32)


def _shift_rows(x, s):
    n = x.shape[0]
    rows = lax.broadcasted_iota(jnp.int32, x.shape, 0)
    y = pltpu.roll(x, s % n, 0)
    if s > 0:
        return jnp.where(rows >= s, y, 0.0)
    return jnp.where(rows < n + s, y, 0.0)


def _rms_bwd(dh, xh, r, g):
    dxh = dh * g
    return r * (dxh - xh * jnp.mean(dxh * xh, axis=-1, keepdims=True))


def _matmul(a, b, mode, out_dtype, name, tm=512, tn=1024, tk=2048):
    if mode == "nn":
        (m, k), (k2, n) = a.shape, b.shape
    elif mode == "nt":
        (m, k), (n, k2) = a.shape, b.shape
    else:
        (k, m), (k2, n) = a.shape, b.shape
    assert k == k2
    tm, tn, tk = min(tm, m), min(tn, n), min(tk, k)
    assert m % tm == 0 and n % tn == 0 and k % tk == 0
    nk = k // tk
    dot = {"nn": _dot, "nt": _dot_nt, "tn": _dot_tn}[mode]

    def body(a_ref, b_ref, o_ref, acc):
        kk = pl.program_id(2)
        part = dot(a_ref[...].astype(BF16), b_ref[...].astype(BF16))
        if nk == 1:
            o_ref[...] = part.astype(out_dtype)
            return

        @pl.when(kk == 0)
        def _():
            acc[...] = part

        @pl.when(kk > 0)
        def _():
            acc[...] += part

        @pl.when(kk == nk - 1)
        def _():
            o_ref[...] = acc[...].astype(out_dtype)

    if mode == "tn":
        a_spec = pl.BlockSpec((tk, tm), lambda i, j, kk: (kk, i))
    else:
        a_spec = pl.BlockSpec((tm, tk), lambda i, j, kk: (i, kk))
    if mode == "nt":
        b_spec = pl.BlockSpec((tn, tk), lambda i, j, kk: (j, kk))
    else:
        b_spec = pl.BlockSpec((tk, tn), lambda i, j, kk: (kk, j))
    return pl.pallas_call(
        body, name=name,
        out_shape=jax.ShapeDtypeStruct((m, n), out_dtype),
        grid=(m // tm, n // tn, nk),
        in_specs=[a_spec, b_spec],
        out_specs=pl.BlockSpec((tm, tn), lambda i, j, kk: (i, j)),
        scratch_shapes=[pltpu.VMEM((tm, tn) if nk > 1 else (8, LANES), F32)],
        compiler_params=_params(dimension_semantics=("parallel", "parallel", "arbitrary")),
    )(a, b)


def _in_proj(x, g1, w_in_t, b_in, after):
    tm = 512

    def body(x_ref, g_ref, w_hbm, b_ref, after_ref, qkv_ref, uy_ref, gg_ref, h_ref, w):
        @pl.when(pl.program_id(0) == 0)
        def _():
            pltpu.sync_copy(w_hbm, w)

        xv = x_ref[...]
        r = lax.rsqrt(jnp.mean(xv * xv, axis=-1, keepdims=True) + EPS)
        h = ((xv * r) * g_ref[...]).astype(BF16)
        h_ref[...] = h
        row0 = 0
        for ref in (qkv_ref, uy_ref, gg_ref):
            for c0 in range(0, ref.shape[1], TILE):
                z = _dot_nt(h, w[row0:row0 + TILE, :]) + b_ref[:, row0:row0 + TILE]
                ref[:, c0:c0 + TILE] = z.astype(ref.dtype)
                row0 += TILE

    tok = lambda width: pl.BlockSpec((tm, width), lambda i: (i, 0))
    return pl.pallas_call(
        body, name="in_proj",
        out_shape=(jax.ShapeDtypeStruct((T, 3 * D_ATT), BF16),
                   jax.ShapeDtypeStruct((T, 2 * D_REC), F32),
                   jax.ShapeDtypeStruct((T, 2 * D), F32),
                   jax.ShapeDtypeStruct((T, D), BF16)),
        grid=(T // tm,),
        in_specs=[tok(D), pl.BlockSpec((1, D), lambda i: (0, 0)), pl.BlockSpec(memory_space=pl.ANY),
                  pl.BlockSpec((1, D_IN), lambda i: (0, 0)), pl.BlockSpec(memory_space=pl.ANY)],
        out_specs=(tok(3 * D_ATT), tok(2 * D_REC), tok(2 * D), tok(D)),
        scratch_shapes=[pltpu.VMEM((D_IN, D), BF16)],
        compiler_params=_params(dimension_semantics=("arbitrary",)),
    )(x, g1, w_in_t, b_in, after)


def _dz_specs(rows, tile_of, row_of):
    def spec(off, n, per_plane):
        def index(*ids):
            t = jnp.clip(tile_of(*ids) - off, 0, n - 1)
            return (t // per_plane, row_of(*ids), t % per_plane)
        return pl.BlockSpec((1, rows, TILE), index)
    return [spec(off, n, per) for off, n, per in DZ_ARRAYS]


def _dh_norm1_bwd(dz, w_in_t, x, g1, dx1, after):
    tm = 512

    def body(dqkv_ref, duy_ref, dgg_ref, w_hbm, x_ref, g_ref, dx1_ref, after_ref, gx_ref, dg_ref, w):
        @pl.when(pl.program_id(0) == 0)
        def _():
            pltpu.sync_copy(w_hbm, w)
            dg_ref[...] = jnp.zeros_like(dg_ref)

        dh, row0 = None, 0
        for ref in (dqkv_ref, duy_ref, dgg_ref):
            for plane in range(ref.shape[0]):
                cols = ref.shape[2]
                part = _dot(ref[plane], w[row0:row0 + cols, :])
                dh = part if dh is None else dh + part
                row0 += cols
        xv = x_ref[...]
        r = lax.rsqrt(jnp.mean(xv * xv, axis=-1, keepdims=True) + EPS)
        xh = xv * r
        dg_ref[...] += jnp.sum(dh * xh, axis=0, keepdims=True)
        gx_ref[...] = dx1_ref[...] + _rms_bwd(dh, xh, r, g_ref[...])

    tok = pl.BlockSpec((tm, D), lambda i: (i, 0))
    vec = pl.BlockSpec((1, D), lambda i: (0, 0))
    planes = lambda a: pl.BlockSpec((a.shape[0], tm, a.shape[2]), lambda i: (0, i, 0))
    return pl.pallas_call(
        body, name="dh_norm1_bwd",
        out_shape=(jax.ShapeDtypeStruct((T, D), F32), jax.ShapeDtypeStruct((1, D), F32)),
        grid=(T // tm,),
        in_specs=[planes(a) for a in dz] + [pl.BlockSpec(memory_space=pl.ANY), tok, vec, tok,
                                            pl.BlockSpec(memory_space=pl.ANY)],
        out_specs=(tok, vec),
        scratch_shapes=[pltpu.VMEM((D_IN, D), BF16)],
        compiler_params=_params(dimension_semantics=("arbitrary",)),
    )(*dz, w_in_t, x, g1, dx1, after)


def _grad_w_in(dz, h):
    def body(*refs):
        seg_refs = refs[:3]
        h_ref, gw_ref, gb_ref = refs[3:]
        j = pl.program_id(0)

        for s, (off, n, _) in enumerate(DZ_ARRAYS):
            @pl.when((j >= off) & (j < off + n))
            def _(s=s):
                a = seg_refs[s][0]
                gw_ref[...] = _dot_tn(a, h_ref[...]).astype(BF16)
                gb_ref[...] = jnp.sum(a.astype(F32), axis=0, keepdims=True)

    return pl.pallas_call(
        body, name="grad_w_in",
        out_shape=(jax.ShapeDtypeStruct((D_IN, D), BF16), jax.ShapeDtypeStruct((1, D_IN), F32)),
        grid=(N_DZ_TILES,),
        in_specs=_dz_specs(T, lambda j: j, lambda j: 0) + [pl.BlockSpec((T, D), lambda j: (0, 0))],
        out_specs=(pl.BlockSpec((TILE, D), lambda j: (j, 0)), pl.BlockSpec((1, TILE), lambda j: (0, j))),
        compiler_params=_params(dimension_semantics=("parallel",)),
    )(*dz, h)


def _rpb_rows(rpb):
    padded = jnp.pad(rpb, ((0, 0), (0, 0), (0, GRID_W - N_RPB_C)))
    rows = [padded[:, WIN_H - 1 - oi: 2 * WIN_H - 1 - oi].reshape(N_HEADS // HG, HG, KWIN)
            for oi in range(WIN_H)]
    return jnp.stack(rows, axis=0)


SKEW = KWIN - (WIN_W - 1)


MASKED = -1e30


def _bias_tiles(rows_ref, valid, bias_s):
    for oi in range(WIN_H):
        for hh in range(HG):
            row = jnp.broadcast_to(rows_ref[oi, 0, hh:hh + 1, :], (GRID_W, KWIN))
            tile = pltpu.roll(row, SKEW, 1, stride=1, stride_axis=0)
            bias_s[oi, hh * GRID_W:(hh + 1) * GRID_W, :] = jnp.where(valid, tile, MASKED)


def _bias_tile_grads(gb_s, flip, out_ref):
    for oi in range(WIN_H):
        for hh in range(HG):
            g = _dot_exact(flip, gb_s[oi, hh * GRID_W:(hh + 1) * GRID_W, :])
            back = pltpu.roll(g, KWIN - (GRID_W - WIN_W), 1, stride=1, stride_axis=0)
            out_ref[0, oi, hh:hh + 1, :] = jnp.sum(back, axis=0, keepdims=True)


def _rpb_fold(row_grads):
    g = row_grads.transpose(1, 0, 2, 3).reshape(WIN_H, N_HEADS, WIN_H, GRID_W)
    g = g.transpose(0, 2, 1, 3)

    def body(g_ref, o_ref):
        for dr in range(N_RPB_R):
            terms = [g_ref[oi, i] for oi in range(WIN_H) for i in range(WIN_H) if i - oi + WIN_H - 1 == dr]
            acc = terms[0]
            for term in terms[1:]:
                acc = acc + term
            o_ref[dr] = acc

    out = pl.pallas_call(
        body, name="rpb_fold",
        out_shape=jax.ShapeDtypeStruct((N_RPB_R, N_HEADS, GRID_W), F32),
    )(g)
    return out.transpose(1, 0, 2)[:, :, :N_RPB_C]


ATT_GROUPS = N_HEADS // HG
ATT_UNROLL = 8


def _stacked(rows64, same_head):
    return jnp.where(same_head, jnp.concatenate([rows64] * HG, axis=0), jnp.zeros((), BF16))


def _own_heads(stacked):
    head = lax.broadcasted_iota(jnp.int32, (GRID_W, HC), 1) // DH
    out = stacked[:GRID_W]
    for h in range(1, HG):
        out = jnp.where(head == h, stacked[h * GRID_W:(h + 1) * GRID_W], out)
    return out


def _att_scores(q_ref, k_ref, bias_ref, same_head, r):
    rs = jnp.clip(r - WIN_H // 2, 0, ROWS - WIN_H)
    oi = r - rs
    q0 = pl.multiple_of(r * GRID_W, GRID_W)
    k0 = pl.multiple_of(rs * GRID_W, GRID_W)
    q2 = _stacked(q_ref[pl.ds(q0, GRID_W), :] * (DH ** -0.5), same_head)
    kw = k_ref[pl.ds(k0, KWIN), :]
    s = _dot_nt(q2, kw) + bias_ref[oi]
    e = jnp.exp(s - jnp.max(s, axis=-1, keepdims=True))
    return e, 1.0 / jnp.sum(e, axis=-1, keepdims=True), q2, kw, q0, k0, oi


def _att_specs():
    col = lambda off: pl.BlockSpec((T, HC), lambda g: (0, g + off * ATT_GROUPS))
    tables = [pl.BlockSpec((WIN_H, 1, HG, KWIN), lambda g: (0, g, 0, 0)),
              pl.BlockSpec((GRID_W, KWIN), lambda g: (0, 0)),
              pl.BlockSpec((HQ, HC), lambda g: (0, 0))]
    return col, tables, pltpu.VMEM((WIN_H, HQ, KWIN), F32)


def _att_fwd(qkv, bias_rows, after):
    valid_np, same_head_np = _att_tables()

    def body(q_ref, k_ref, v_ref, rows_ref, valid_ref, head_ref, after_ref, o_ref, bias_s):
        same_head = head_ref[...] > 0.5
        _bias_tiles(rows_ref, valid_ref[...] > 0.5, bias_s)

        def row(r, carry):
            e, rl, _, _, q0, k0, _ = _att_scores(q_ref, k_ref, bias_s, same_head, r)
            o2 = _dot((e * rl).astype(BF16), v_ref[pl.ds(k0, KWIN), :])
            o_ref[pl.ds(q0, GRID_W), :] = _own_heads(o2).astype(BF16)
            return carry

        lax.fori_loop(0, ROWS, row, 0, unroll=ATT_UNROLL)

    col, tables, tiles = _att_specs()
    return pl.pallas_call(
        body, name="att_fwd",
        out_shape=jax.ShapeDtypeStruct((T, D_ATT), BF16),
        grid=(ATT_GROUPS,),
        in_specs=[col(0), col(1), col(2)] + tables + [pl.BlockSpec(memory_space=pl.ANY)],
        out_specs=col(0),
        scratch_shapes=[tiles],
        compiler_params=_params(dimension_semantics=("parallel",)),
    )(qkv, qkv, qkv, bias_rows, jnp.asarray(valid_np), jnp.asarray(same_head_np), after)


def _att_bwd(qkv, bias_rows, datt, after):
    valid_np, same_head_np = _att_tables()

    def body(q_ref, k_ref, v_ref, do_ref, rows_ref, valid_ref, head_ref, flip_ref, after_ref,
             dqkv_ref, grows_ref, dk_acc, dv_acc, bias_s, gb_s):
        same_head = head_ref[...] > 0.5
        dk_acc[...] = jnp.zeros_like(dk_acc)
        dv_acc[...] = jnp.zeros_like(dv_acc)
        gb_s[...] = jnp.zeros_like(gb_s)
        _bias_tiles(rows_ref, valid_ref[...] > 0.5, bias_s)

        def row(r, carry):
            e, rl, q2, kw, q0, k0, oi = _att_scores(q_ref, k_ref, bias_s, same_head, r)
            do2 = _stacked(do_ref[pl.ds(q0, GRID_W), :], same_head)
            vw = v_ref[pl.ds(k0, KWIN), :]
            p = e * rl
            dp = _dot_nt(do2, vw)
            ds = p * (dp - jnp.sum(dp * p, axis=-1, keepdims=True))
            p16 = p.astype(BF16)
            ds16 = ds.astype(BF16)
            dv_acc[pl.ds(k0, KWIN), :] += _dot_tn(p16, do2)
            dk_acc[pl.ds(k0, KWIN), :] += _dot_tn(ds16, q2)
            dq2 = _dot(ds16, kw) * (DH ** -0.5)
            dqkv_ref[0, pl.ds(q0, GRID_W), :] = _own_heads(dq2).astype(BF16)
            gb_s[oi] += ds
            return carry

        lax.fori_loop(0, ROWS, row, 0, unroll=ATT_UNROLL)
        dqkv_ref[1] = dk_acc[...].astype(BF16)
        dqkv_ref[2] = dv_acc[...].astype(BF16)
        _bias_tile_grads(gb_s, flip_ref[...], grows_ref)

    col, tables, tiles = _att_specs()
    return pl.pallas_call(
        body, name="att_bwd",
        out_shape=(jax.ShapeDtypeStruct((3, T, D_ATT), BF16),
                   jax.ShapeDtypeStruct((ATT_GROUPS, WIN_H, HG, KWIN), F32)),
        grid=(ATT_GROUPS,),
        in_specs=[col(0), col(1), col(2), col(0)] + tables + [pl.BlockSpec((GRID_W, GRID_W), lambda g: (0, 0)),
                                                              pl.BlockSpec(memory_space=pl.ANY)],
        out_specs=(pl.BlockSpec((3, T, HC), lambda g: (0, 0, g)),
                   pl.BlockSpec((1, WIN_H, HG, KWIN), lambda g: (g, 0, 0, 0))),
        scratch_shapes=[pltpu.VMEM((T, HC), F32), pltpu.VMEM((T, HC), F32), tiles, tiles],
        compiler_params=_params(dimension_semantics=("parallel",)),
    )(qkv, qkv, qkv, datt, bias_rows, jnp.asarray(valid_np), jnp.asarray(same_head_np),
      jnp.asarray(np.eye(GRID_W, dtype=np.float32)[::-1].copy()), after)


def _conv_taps(up):
    return (_shift_rows(up, 2), _shift_rows(up, 1), up, _shift_rows(up, -1))


def _pair_block_diag(w_pair, dup, same_half):
    return jnp.where(same_half, _dot(w_pair.astype(BF16), dup), 0.0).astype(BF16)


def _gates(u, u16, wa, ba, wi, bi, lam):
    r = _sigmoid(_dot(u16, wa) + ba)
    ig = _sigmoid(_dot(u16, wi) + bi)
    sp = _softplus(-lam)
    log_a = (-LRU_C) * r * sp
    a = jnp.exp(log_a)
    mult2 = jnp.maximum(_one_minus_square(log_a, a), 0.0)
    return r, ig, sp, a, jnp.sqrt(mult2), mult2


SCAN_BLOCKS = 8


def _scans(jobs):
    c = jobs[0][0].shape[1]
    nblk = T // 8
    rows = lax.broadcasted_iota(jnp.int32, (8, c), 0)

    def block(a, b, reverse):
        for s in (1, 2, 4):
            if reverse:
                keep = rows < 8 - s
                a_s = jnp.where(keep, pltpu.roll(a, 8 - s, 0), 1.0)
                b_s = jnp.where(keep, pltpu.roll(b, 8 - s, 0), 0.0)
            else:
                keep = rows >= s
                a_s = jnp.where(keep, pltpu.roll(a, s, 0), 1.0)
                b_s = jnp.where(keep, pltpu.roll(b, s, 0), 0.0)
            b = a * b_s + b
            a = a * a_s
        return a, b

    def step(i, carry):
        out = []
        for (a_ref, b_ref, h_ref, reverse), h_prev in zip(jobs, carry):
            for u in range(SCAN_BLOCKS):
                blk = i * SCAN_BLOCKS + u
                if reverse:
                    blk = nblk - 1 - blk
                t0 = pl.multiple_of(blk * 8, 8)
                a, b = block(a_ref[pl.ds(t0, 8), :], b_ref[pl.ds(t0, 8), :], reverse)
                h = a * h_prev + b
                h_ref[pl.ds(t0, 8), :] = h
                h_prev = jnp.broadcast_to(h[0:1] if reverse else h[7:8], (8, c))
            out.append(h_prev)
        return tuple(out)

    lax.fori_loop(0, nblk // SCAN_BLOCKS, step, tuple(jnp.zeros((8, c), F32) for _ in jobs))


def _rec_specs():
    tok = lambda off: pl.BlockSpec((T, CG), lambda g: (0, g + off))
    per_ch = lambda rows: pl.BlockSpec((rows, CG), lambda g: (0, g))
    wspec = pl.BlockSpec((2, 1, CG, REC_BLOCK), lambda g: (0, g, 0, 0))
    const = lambda shape: pl.BlockSpec(shape, lambda g: (0, 0))
    return tok, per_ch, wspec, const


def _rec_fwd(uy, conv_w, conv_b, w_a, b_a, w_i, b_i, lam):
    tok, per_ch, wspec, const = _rec_specs()

    def body(up_ref, yb_ref, cw_ref, cb_ref, wa_ref, ba_ref, wi_ref, bi_ref, lam_ref, dup_ref, half_ref,
             hf_ref, hb_ref, yrec_ref, am_ref, bx_f, bx_b):
        dup = dup_ref[...]
        same_half = half_ref[...] > 0.5
        taps = _conv_taps(up_ref[...])
        u = cb_ref[...]
        for j in range(4):
            u = u + taps[j] * cw_ref[j:j + 1, :]
        u16 = u.astype(BF16)
        for d, bx_s in enumerate((bx_f, bx_b)):
            wa = _pair_block_diag(wa_ref[d, 0], dup, same_half)
            wi = _pair_block_diag(wi_ref[d, 0], dup, same_half)
            _, ig, _, a, mult, _ = _gates(u, u16, wa, ba_ref[d:d + 1, :], wi, bi_ref[d:d + 1, :],
                                       lam_ref[d:d + 1, :])
            am_ref[2 * d] = a
            am_ref[2 * d + 1] = mult
            bx_s[...] = mult * (ig * u)
        _scans([(am_ref.at[0], bx_f, hf_ref, False), (am_ref.at[2], bx_b, hb_ref, True)])
        gelu, _ = _gelu_and_grad(yb_ref[...])
        yrec_ref[...] = ((hf_ref[...] + hb_ref[...]) * gelu).astype(BF16)

    return pl.pallas_call(
        body, name="rec_fwd",
        out_shape=(jax.ShapeDtypeStruct((T, D_REC), F32), jax.ShapeDtypeStruct((T, D_REC), F32),
                   jax.ShapeDtypeStruct((T, D_REC), BF16), jax.ShapeDtypeStruct((4, T, D_REC), F32)),
        grid=(N_CG,),
        in_specs=[tok(0), tok(N_CG), per_ch(4), per_ch(1), wspec, per_ch(2), wspec, per_ch(2), per_ch(2),
                  const((REC_BLOCK, CG)), const((CG, CG))],
        out_specs=(tok(0), tok(0), tok(0), pl.BlockSpec((4, T, CG), lambda g: (0, 0, g))),
        scratch_shapes=[pltpu.VMEM((T, CG), F32)] * 2,
        compiler_params=_params(dimension_semantics=("parallel",)),
    )(uy, uy, conv_w, conv_b, w_a, b_a, w_i, b_i, lam,
      jnp.asarray(_dup_table(), BF16), jnp.asarray(_pair_mask()))


def _rec_bwd(uy, hf, hb, am, dyrec, conv_w, conv_b, w_a, b_a, w_i, b_i, lam, after):
    tok, per_ch, wspec, const = _rec_specs()

    def body(up_ref, yb_ref, hf_ref, hb_ref, am_ref, dy_ref, cw_ref, cb_ref, wa_ref, ba_ref, wi_ref, bi_ref,
             lam_ref, dup_ref, dupt_ref, half_ref, after_ref,
             duy_ref, dcw_ref, dcb_ref, dwa_ref, dba_ref, dwi_ref, dbi_ref, dlam_ref,
             a_s0, a_s1, dh_s, g_s0, g_s1):
        dup = dup_ref[...]
        dup_t = dupt_ref[...]
        same_half = half_ref[...] > 0.5
        taps = _conv_taps(up_ref[...])
        u = cb_ref[...]
        for j in range(4):
            u = u + taps[j] * cw_ref[j:j + 1, :]
        u16 = u.astype(BF16)
        gelu, dgelu = _gelu_and_grad(yb_ref[...])
        dy = dy_ref[...]
        duy_ref[1] = (dy * (hf_ref[...] + hb_ref[...]) * dgelu).astype(BF16)
        dh_s[...] = dy * gelu
        a_s0[...] = _shift_rows(am_ref[0], -1)
        a_s1[...] = _shift_rows(am_ref[2], 1)
        _scans([(a_s0, dh_s, g_s0, True), (a_s1, dh_s, g_s1, False)])
        du = jnp.zeros((T, CG), F32)
        for d, g_s in enumerate((g_s0, g_s1)):
            reverse = d == 1
            wa = _pair_block_diag(wa_ref[d, 0], dup, same_half)
            wi = _pair_block_diag(wi_ref[d, 0], dup, same_half)
            lam_d = lam_ref[d:d + 1, :]
            r = _sigmoid(_dot(u16, wa) + ba_ref[d:d + 1, :])
            ig = _sigmoid(_dot(u16, wi) + bi_ref[d:d + 1, :])
            sp = _softplus(-lam_d)
            a, mult = am_ref[2 * d], am_ref[2 * d + 1]
            mult2 = mult * mult
            g = g_s[...]
            h_prev = _shift_rows(hb_ref[...], -1) if reverse else _shift_rows(hf_ref[...], 1)
            da = g * h_prev
            dmult = g * (ig * u)
            dig = g * mult * u
            du = du + g * mult * ig
            dmult_dlog = jnp.where(mult2 > 0.0, -(a * a) * lax.rsqrt(mult2), 0.0)
            dlog_a = da * a + dmult * dmult_dlog
            dr = dlog_a * ((-LRU_C) * sp)
            dsp = jnp.sum(dlog_a * ((-LRU_C) * r), axis=0, keepdims=True)
            dlam_ref[d:d + 1, :] = dsp * (-_sigmoid(-lam_d))
            dga = dr * r * (1.0 - r)
            dgi = dig * ig * (1.0 - ig)
            dga16 = dga.astype(BF16)
            dgi16 = dgi.astype(BF16)
            du = du + _dot_nt(dga16, wa) + _dot_nt(dgi16, wi)
            dwa_ref[d, 0] = _dot_exact(jnp.where(same_half, _dot_tn(u16, dga16), 0.0), dup_t)
            dwi_ref[d, 0] = _dot_exact(jnp.where(same_half, _dot_tn(u16, dgi16), 0.0), dup_t)
            dba_ref[d:d + 1, :] = jnp.sum(dga, axis=0, keepdims=True)
            dbi_ref[d:d + 1, :] = jnp.sum(dgi, axis=0, keepdims=True)
        dcb_ref[...] = jnp.sum(du, axis=0, keepdims=True)
        for j in range(4):
            dcw_ref[j:j + 1, :] = jnp.sum(du * taps[j], axis=0, keepdims=True)
        dup_in = (_shift_rows(du, -2) * cw_ref[0:1, :] + _shift_rows(du, -1) * cw_ref[1:2, :]
                  + du * cw_ref[2:3, :] + _shift_rows(du, 1) * cw_ref[3:4, :])
        duy_ref[0] = dup_in.astype(BF16)

    wshape = jax.ShapeDtypeStruct((2, N_CG, CG, REC_BLOCK), F32)
    vec = lambda rows: jax.ShapeDtypeStruct((rows, D_REC), F32)
    dup_np = _dup_table()
    return pl.pallas_call(
        body, name="rec_bwd",
        out_shape=(jax.ShapeDtypeStruct((2, T, D_REC), BF16),
                   vec(4), vec(1), wshape, vec(2), wshape, vec(2), vec(2)),
        grid=(N_CG,),
        in_specs=[tok(0), tok(N_CG), tok(0), tok(0), pl.BlockSpec((4, T, CG), lambda g: (0, 0, g)), tok(0),
                  per_ch(4), per_ch(1), wspec, per_ch(2), wspec, per_ch(2), per_ch(2),
                  const((REC_BLOCK, CG)), const((CG, REC_BLOCK)), const((CG, CG)),
                  pl.BlockSpec(memory_space=pl.ANY)],
        out_specs=(pl.BlockSpec((2, T, CG), lambda g: (0, 0, g)),
                   per_ch(4), per_ch(1), wspec, per_ch(2), wspec, per_ch(2), per_ch(2)),
        scratch_shapes=[pltpu.VMEM((T, CG), F32)] * 5,
        compiler_params=_params(dimension_semantics=("parallel",)),
    )(uy, uy, hf, hb, am, dyrec, conv_w, conv_b, w_a, b_a, w_i, b_i, lam,
      jnp.asarray(dup_np, BF16), jnp.asarray(dup_np.T.copy()), jnp.asarray(_pair_mask()), after)


TM_MIX = 256


def _mix_specs():
    tok = lambda width, blk=0: pl.BlockSpec((TM_MIX, width), lambda i: (i, blk))
    full = lambda shape: pl.BlockSpec(shape, lambda i: (0, 0))
    return tok, full


def _mix_fwd(x, att, yrec, gg, w_att_o_t, w_rec_o, w_out):
    tok, full = _mix_specs()

    def body(x_ref, att_ref, yr_ref, ga_ref, gr_ref, wao_ref, wro_ref, wo_ref, x1_ref, mixed_ref):
        y_att = _dot_nt(att_ref[...], wao_ref[...])
        y_rec = _dot(yr_ref[...], wro_ref[...])
        mixed = (_sigmoid(ga_ref[...]) * y_att + _sigmoid(gr_ref[...]) * y_rec).astype(BF16)
        mixed_ref[...] = mixed
        x1_ref[...] = x_ref[...] + _dot(mixed, wo_ref[...])

    return pl.pallas_call(
        body, name="mix_fwd",
        out_shape=(jax.ShapeDtypeStruct((T, D), F32), jax.ShapeDtypeStruct((T, D), BF16)),
        grid=(T // TM_MIX,),
        in_specs=[tok(D), tok(D_ATT), tok(D_REC), tok(D, 0), tok(D, 1),
                  full((D, D_ATT)), full((D_REC, D)), full((D, D))],
        out_specs=(tok(D), tok(D)),
        compiler_params=_params(dimension_semantics=("parallel",)),
    )(x, att, yrec, gg, gg, w_att_o_t, w_rec_o, w_out)


def _mix_bwd(dx1, att, yrec, gg, w_att_o_t, w_rec_o, w_out, after):
    tok, full = _mix_specs()

    def body(dx_ref, att_ref, yr_ref, ga_ref, gr_ref, wao_ref, wro_ref, wo_ref, after_ref,
             dgg_ref, dya_ref, dyr_ref, datt_ref, dyrp_ref):
        dmixed = _dot_nt(dx_ref[...].astype(BF16), wo_ref[...])
        y_att = _dot_nt(att_ref[...], wao_ref[...])
        y_rec = _dot(yr_ref[...], wro_ref[...])
        sa = _sigmoid(ga_ref[...])
        sr = _sigmoid(gr_ref[...])
        dgg_ref[0] = (dmixed * y_att * sa * (1.0 - sa)).astype(BF16)
        dgg_ref[1] = (dmixed * y_rec * sr * (1.0 - sr)).astype(BF16)
        dya = (dmixed * sa).astype(BF16)
        dyr = (dmixed * sr).astype(BF16)
        dya_ref[...] = dya
        dyr_ref[...] = dyr
        datt_ref[...] = _dot(dya, wao_ref[...]).astype(BF16)
        dyrp_ref[...] = _dot_nt(dyr, wro_ref[...])

    return pl.pallas_call(
        body, name="mix_bwd",
        out_shape=(jax.ShapeDtypeStruct((2, T, D), BF16),
                   jax.ShapeDtypeStruct((T, D), BF16), jax.ShapeDtypeStruct((T, D), BF16),
                   jax.ShapeDtypeStruct((T, D_ATT), BF16), jax.ShapeDtypeStruct((T, D_REC), F32)),
        grid=(T // TM_MIX,),
        in_specs=[tok(D), tok(D_ATT), tok(D_REC), tok(D, 0), tok(D, 1),
                  full((D, D_ATT)), full((D_REC, D)), full((D, D)), pl.BlockSpec(memory_space=pl.ANY)],
        out_specs=(pl.BlockSpec((2, TM_MIX, D), lambda i: (0, i, 0)),
                   tok(D), tok(D), tok(D_ATT), tok(D_REC)),
        compiler_params=_params(dimension_semantics=("parallel",)),
    )(dx1, att, yrec, gg, gg, w_att_o_t, w_rec_o, w_out, after)


TM_FFN = 256
FF_CHUNK = 1024


def _ffn_loss(x1, target, g2, gf, w_ff1_t, w_ff2):
    n_chunks = D_FF // FF_CHUNK

    def body(x1_ref, tg_ref, g2_ref, gf_ref, w1_hbm, w2_hbm,
             loss_ref, dx1_ref, h2_ref, act_ref, dpre_ref, dx2_ref, dg2_ref, dgf_ref,
             w1, w2, relu_s):
        i = pl.program_id(0)

        @pl.when(i == 0)
        def _():
            pltpu.sync_copy(w1_hbm, w1)
            pltpu.sync_copy(w2_hbm, w2)
            loss_ref[...] = jnp.zeros_like(loss_ref)
            dg2_ref[...] = jnp.zeros_like(dg2_ref)
            dgf_ref[...] = jnp.zeros_like(dgf_ref)

        x1v = x1_ref[...]
        r2 = lax.rsqrt(jnp.mean(x1v * x1v, axis=-1, keepdims=True) + EPS)
        xh2 = x1v * r2
        h2 = (xh2 * g2_ref[...]).astype(BF16)
        h2_ref[...] = h2
        x2 = x1v
        for c in range(n_chunks):
            ff = slice(c * FF_CHUNK, (c + 1) * FF_CHUNK)
            rl = jnp.maximum(_dot_nt(h2, w1[ff, :]), 0.0)
            relu_s[:, ff] = rl
            act = (rl * rl).astype(BF16)
            act_ref[:, ff] = act
            x2 = x2 + _dot(act, w2[ff, :])
        r3 = lax.rsqrt(jnp.mean(x2 * x2, axis=-1, keepdims=True) + EPS)
        xh3 = x2 * r3
        err = xh3 * gf_ref[...] - tg_ref[...]
        loss_ref[...] += 0.5 * jnp.sum(jnp.mean(err * err, axis=-1, keepdims=True))
        dy = err * (1.0 / D)
        dgf_ref[...] += jnp.sum(dy * xh3, axis=0, keepdims=True)
        dx2 = _rms_bwd(dy, xh3, r3, gf_ref[...])
        dx2_16 = dx2.astype(BF16)
        dx2_ref[...] = dx2_16
        dh2 = jnp.zeros((TM_FFN, D), F32)
        for c in range(n_chunks):
            ff = slice(c * FF_CHUNK, (c + 1) * FF_CHUNK)
            dpre = (_dot_nt(dx2_16, w2[ff, :]) * (2.0 * relu_s[:, ff])).astype(BF16)
            dpre_ref[:, ff] = dpre
            dh2 = dh2 + _dot(dpre, w1[ff, :])
        dg2_ref[...] += jnp.sum(dh2 * xh2, axis=0, keepdims=True)
        dx1_ref[...] = dx2 + _rms_bwd(dh2, xh2, r2, g2_ref[...])

    tok = lambda width: pl.BlockSpec((TM_FFN, width), lambda i: (i, 0))
    vec = pl.BlockSpec((1, D), lambda i: (0, 0))
    hbm = pl.BlockSpec(memory_space=pl.ANY)
    return pl.pallas_call(
        body, name="ffn_loss",
        out_shape=(jax.ShapeDtypeStruct((8, 128), F32), jax.ShapeDtypeStruct((T, D), F32),
                   jax.ShapeDtypeStruct((T, D), BF16), jax.ShapeDtypeStruct((T, D_FF), BF16),
                   jax.ShapeDtypeStruct((T, D_FF), BF16), jax.ShapeDtypeStruct((T, D), BF16),
                   jax.ShapeDtypeStruct((1, D), F32), jax.ShapeDtypeStruct((1, D), F32)),
        grid=(T // TM_FFN,),
        in_specs=[tok(D), tok(D), vec, vec, hbm, hbm],
        out_specs=(pl.BlockSpec((8, 128), lambda i: (0, 0)), tok(D), tok(D), tok(D_FF), tok(D_FF), tok(D),
                   vec, vec),
        scratch_shapes=[pltpu.VMEM((D_FF, D), BF16), pltpu.VMEM((D_FF, D), BF16),
                        pltpu.VMEM((TM_FFN, D_FF), F32)],
        compiler_params=_params(dimension_semantics=("arbitrary",)),
    )(x1, target, g2, gf, w_ff1_t, w_ff2)


def _local_step(x, target, p, late_weights, late_weights_ready, reduce_first, reduce_early, reduce_late):
    bias = _rpb_rows(p["rpb"])
    pairs = lambda w: w.reshape(2, N_CG, CG, REC_BLOCK)
    w_a, w_i = pairs(p["w_rg_a"]), pairs(p["w_rg_i"])
    rec_params = (p["conv_w"], p["conv_b"], w_a, p["b_rg_a"], w_i, p["b_rg_i"], p["lru_lambda"])

    qkv, uy, gg, h = _in_proj(x, p["ln1_g"], p["w_in_t"], p["b_in"], p["later_weights_started"])
    hf, hb, yrec, am = _rec_fwd(uy, *rec_params)
    att = _att_fwd(qkv, bias, late_weights(yrec, 0))
    p = {**p, **late_weights_ready(late_weights(att, 1), 0)}
    x1, mixed = _mix_fwd(x, att, yrec, gg, p["w_att_o_t"], p["w_rec_o"], p["w_out"])
    p = {**p, **late_weights_ready(x1, 1)}
    loss8, dx1, h2, act, dpre, dx2, g_ln2, g_lnf = _ffn_loss(
        x1, target, p["ln2_g"], p["lnf_g"], p["w_ff1_t"], p["w_ff2"])

    grads = {"ln2_g": g_ln2, "lnf_g": g_lnf,
             "w_ff1_t": _matmul(dpre, h2, "tn", BF16, "g_w_ff1"),
             "w_ff2": _matmul(act, dx2, "tn", BF16, "g_w_ff2")}
    dgg, dya, dyr, datt, dyrp = _mix_bwd(dx1, att, yrec, gg, p["w_att_o_t"], p["w_rec_o"], p["w_out"],
                                         reduce_first(grads, None))
    duy, g_cw, g_cb, g_wa, g_ba, g_wi, g_bi, g_lam = _rec_bwd(uy, hf, hb, am, dyrp, *rec_params,
                                                              reduce_first(None, dgg))
    blocks = lambda g: g.reshape(2, N_REC_BLOCKS, REC_BLOCK, REC_BLOCK)
    grads.update({
        "w_att_o_t": _matmul(dya, att, "tn", BF16, "g_w_att_o"),
        "conv_w": g_cw, "conv_b": g_cb, "w_rg_a": blocks(g_wa), "b_rg_a": g_ba,
        "w_rg_i": blocks(g_wi), "b_rg_i": g_bi, "lru_lambda": g_lam,
        "w_rec_o": _matmul(yrec, dyr, "tn", BF16, "g_w_rec_o"),
        "w_out": _matmul(mixed, dx1, "tn", BF16, "g_w_out"),
    })
    dqkv, gbias = _att_bwd(qkv, bias, datt, reduce_early(grads))
    dz = (dqkv, duy, dgg)
    g_w_in_t, g_b_in = _grad_w_in(dz, h)
    grads.update(w_in_t=g_w_in_t, b_in=g_b_in)
    grad_x, g_ln1 = _dh_norm1_bwd(dz, p["w_in_t"], x, p["ln1_g"], dx1, reduce_late(grads))
    grads.update(ln1_g=g_ln1, rpb=_rpb_fold(gbias))
    return loss8[0:1, 0:1], grad_x, grads


MESH_ID = pl.DeviceIdType.MESH
ANY = pl.BlockSpec(memory_space=pl.ANY)

CHAN_BLOCK_ROWS = 32
GATE_ROWS = 2 * 2 * N_REC_BLOCKS * REC_BLOCK * REC_BLOCK // (N_DEV * D)
SECTIONS = (("w_in_t", 704, D), ("w_rec_o", 128, D), ("w_out", 128, D), ("w_ff1_t", 512, D),
            ("w_ff2", 512, D), ("chan", CHAN_BLOCK_ROWS, D), ("w_att_o_t", 128, D_ATT),
            ("gates", GATE_ROWS, D))
N_SEC = len(SECTIONS)
N_CHAN_ROWS = 10
CHAN = (("conv_w", 4), ("b_rg_a", 2), ("b_rg_i", 2), ("lru_lambda", 2))


def _position():
    return lax.axis_index("x"), lax.axis_index("y"), lax.axis_index("c")


def _other_chips(x, y):
    return [(1 - x, y), (x, 1 - y), (1 - x, 1 - y)]


PASS_ON_IDS, PAIR_EARLY_ID, PAIR_LATE_ID, PAIR_FIRST_ID, SMALL_PASS_ON_ID = (1, 4), 2, 3, 5, 6


def _pair_handshake(x, y, c):
    barrier = pltpu.get_barrier_semaphore()
    pl.semaphore_signal(barrier, inc=1, device_id=(x, y, 1 - c), device_id_type=MESH_ID)
    pl.semaphore_wait(barrier, 1)


def _block_of(ref, dev, rows):
    return ref.at[pl.ds(pl.multiple_of(dev * rows, 16), rows)]


def _all_gather(shards, name):
    ns = len(shards)

    def body(*refs):
        x_refs, out_refs = refs[:ns], refs[ns:2 * ns]
        send_sems, recv_sems, local_sems = refs[2 * ns:]
        x, y, c = _position()
        me, sibling = (x, y, c), (x, y, 1 - c)
        x_nbr, y_nbr, diagonal = _other_chips(x, y)
        north = c == 1
        relay_from = (jnp.where(north, x_nbr[0], y_nbr[0]), jnp.where(north, x_nbr[1], y_nbr[1]))
        relay_to = (jnp.where(north, y_nbr[0], x_nbr[0]), jnp.where(north, y_nbr[1], x_nbr[1]))

        def rows(s, px, py, pc):
            return _block_of(out_refs[s], 4 * px + 2 * py + pc, shards[s].shape[0])

        def copy(k, s, block, to, from_shard=False):
            return pltpu.make_async_remote_copy(
                src_ref=x_refs[s] if from_shard else rows(s, *block), dst_ref=rows(s, *block),
                send_sem=send_sems.at[k * ns + s], recv_sem=recv_sems.at[k * ns + s],
                device_id=to, device_id_type=MESH_ID)

        sections = range(ns)
        mine = [pltpu.make_async_copy(x_refs[s], rows(s, *me), local_sems.at[s]) for s in sections]
        sent = [copy(k, s, me, to, True) for k, to in enumerate((sibling, (*x_nbr, c), (*y_nbr, c)))
                for s in sections]
        for cp in mine + sent:
            cp.start()
        for s in sections:
            copy(1, s, (*x_nbr, c), me).wait_recv()
            copy(2, s, (*y_nbr, c), me).wait_recv()
            sent += [copy(3, s, (*relay_from, c), (*relay_to, c)),
                     copy(4, s, (*x_nbr, c), sibling), copy(5, s, (*y_nbr, c), sibling)]
            for cp in sent[-3:]:
                cp.start()
        for s in sections:
            copy(3, s, (*diagonal, c), me).wait_recv()
            sent.append(copy(6, s, (*diagonal, c), sibling))
            sent[-1].start()
        for s in sections:
            copy(0, s, sibling, me).wait_recv()
            for k, chip in ((4, x_nbr), (5, y_nbr), (6, diagonal)):
                copy(k, s, (*chip, 1 - c), me).wait_recv()
        for cp in sent:
            cp.wait_send()
        for cp in mine:
            cp.wait()

    return pl.pallas_call(
        body, name=name,
        out_shape=tuple(jax.ShapeDtypeStruct((N_DEV * s.shape[0], s.shape[1]), s.dtype) for s in shards),
        in_specs=[ANY] * ns,
        out_specs=(ANY,) * ns,
        scratch_shapes=[pltpu.SemaphoreType.DMA((7 * ns,)), pltpu.SemaphoreType.DMA((7 * ns,)),
                        pltpu.SemaphoreType.DMA((ns,))],
    )(*shards)


HBM = pl.BlockSpec(memory_space=pltpu.HBM)
SEM = pl.BlockSpec(memory_space=pltpu.SEMAPHORE)
EFFECT = pltpu.SideEffectType.DATAFLOW_SIDE_EFFECTING


def _in_hbm(a):
    return pltpu.with_memory_space_constraint(a, pltpu.HBM)


def _first_hop_copies(rows, which, x_refs, zones, send_sems, recv_sems):
    ns = len(rows)
    x, y, c = _position()
    targets = [(x, y, 1 - c)] + [(cx, cy, c) for cx, cy in _other_chips(x, y)]
    return [pltpu.make_async_remote_copy(
        src_ref=x_refs[i], dst_ref=_block_of(zones[i], 4 * x + 2 * y + c, rows[s]),
        send_sem=send_sems.at[k * ns + s], recv_sem=recv_sems.at[k * ns + s],
        device_id=to, device_id_type=MESH_ID)
        for k, to in enumerate(targets) for i, s in enumerate(which)]


def _after_all(arrays, name):
    def body(*refs):
        refs[-1][...] = jnp.zeros_like(refs[-1])

    return pl.pallas_call(
        body, name=name,
        out_shape=jax.ShapeDtypeStruct((8, LANES), F32),
        in_specs=[pl.BlockSpec(memory_space=pl.ANY)] * len(arrays),
        out_specs=pl.BlockSpec(memory_space=pltpu.VMEM),
    )(*arrays)


def _own_blocks_placed(shards, after, name):
    ns = len(shards)
    x, y, c = _position()
    me = jnp.reshape(4 * x + 2 * y + c, (1,)).astype(jnp.int32)
    tokens = [] if after is None else [after]

    def body(me_ref, *refs):
        for s in range(ns):
            refs[ns + len(tokens) + s][...] = refs[s][...]

    return pl.pallas_call(
        body, name=name,
        out_shape=tuple(jax.ShapeDtypeStruct((N_DEV * s.shape[0], s.shape[1]), s.dtype) for s in shards),
        grid_spec=pltpu.PrefetchScalarGridSpec(
            num_scalar_prefetch=1, grid=(1,),
            in_specs=[pl.BlockSpec(s.shape, lambda i, me: (0, 0)) for s in shards] + [ANY] * len(tokens),
            out_specs=tuple(pl.BlockSpec(s.shape, lambda i, me: (me[0], 0)) for s in shards)),
        compiler_params=_params(dimension_semantics=("arbitrary",)),
    )(me, *shards, *tokens)


def _gather_start(shards, after, name):
    ns = len(shards)
    zones = _own_blocks_placed(shards, after, name + "_own_blocks")

    def body(*refs):
        for cp in _first_hop_copies([s.shape[0] for s in shards], range(ns), refs[:ns], refs[ns:2 * ns],
                                    refs[2 * ns], refs[2 * ns + 1]):
            cp.start()
        refs[-1][...] = jnp.zeros_like(refs[-1])

    out = pl.pallas_call(
        body, name=name,
        out_shape=(pltpu.SemaphoreType.DMA((4 * ns,)), pltpu.SemaphoreType.DMA((4 * ns,)),
                   *[pltpu.HBM(a.shape, a.dtype) for a in (*shards, *zones)],
                   jax.ShapeDtypeStruct((8, LANES), F32)),
        in_specs=[HBM] * (2 * ns),
        out_specs=(SEM, SEM, *[HBM] * (2 * ns), pl.BlockSpec(memory_space=pltpu.VMEM)),
        input_output_aliases={i: 2 + i for i in range(2 * ns)},
        compiler_params=pltpu.CompilerParams(has_side_effects=EFFECT),
    )(*[_in_hbm(a) for a in shards], *[_in_hbm(a) for a in zones])
    return out[0], out[1], out[2:2 + ns], out[2 + ns:2 + 2 * ns], out[-1]


def _gather_wait(send_sems, recv_sems, rows, which, shards, zones, after, name):
    ns = len(shards)

    def body(*refs):
        for cp in _first_hop_copies(rows, which, refs[:ns], refs[ns:2 * ns], refs[2 * ns], refs[2 * ns + 1]):
            cp.wait_send()
            cp.wait_recv()

    out = pl.pallas_call(
        body, name=name,
        out_shape=tuple(pltpu.HBM(a.shape, a.dtype) for a in (*shards, *zones)),
        in_specs=[HBM] * (2 * ns) + [SEM, SEM, ANY],
        out_specs=(HBM,) * (2 * ns),
        input_output_aliases={i: i for i in range(2 * ns)},
        compiler_params=pltpu.CompilerParams(has_side_effects=EFFECT),
    )(*shards, *zones, send_sems, recv_sems, after)
    return out[:ns], out[ns:]


def _pass_on_copies(rows, in_refs, out_refs, send_sems, recv_sems):
    ns = len(rows)
    x, y, c = _position()
    return [pltpu.make_async_remote_copy(
        src_ref=_block_of(in_refs[s], 4 * cx + 2 * cy + c, rows[s]),
        dst_ref=_block_of(out_refs[s], 4 * cx + 2 * cy + c, rows[s]),
        send_sem=send_sems.at[j * ns + s], recv_sem=recv_sems.at[j * ns + s],
        device_id=(x, y, 1 - c), device_id_type=MESH_ID)
        for j, (cx, cy) in enumerate(_other_chips(x, y)) for s in range(ns)]


def _pass_on_start(rows, zones, barrier_id, name):
    ns = len(zones)

    def body(*refs):
        _pair_handshake(*_position())
        for cp in _pass_on_copies(rows, refs[:ns], refs[:ns], refs[ns], refs[ns + 1]):
            cp.start()
        refs[-1][...] = jnp.zeros_like(refs[-1])

    out = pl.pallas_call(
        body, name=name,
        out_shape=(pltpu.SemaphoreType.DMA((3 * ns,)), pltpu.SemaphoreType.DMA((3 * ns,)),
                   *[pltpu.HBM(z.shape, z.dtype) for z in zones], jax.ShapeDtypeStruct((8, LANES), F32)),
        in_specs=[HBM] * ns,
        out_specs=(SEM, SEM, *[HBM] * ns, pl.BlockSpec(memory_space=pltpu.VMEM)),
        input_output_aliases={i: 2 + i for i in range(ns)},
        compiler_params=pltpu.CompilerParams(has_side_effects=EFFECT, collective_id=barrier_id),
    )(*[_in_hbm(z) for z in zones])
    return out[0], out[1], out[2:2 + ns], out[-1]


def _pass_on_wait(rows, send_sems, recv_sems, zones, after, name):
    ns = len(zones)

    def body(*refs):
        for cp in _pass_on_copies(rows, refs[:ns], refs[:ns], refs[ns], refs[ns + 1]):
            cp.wait_send()
            cp.wait_recv()

    return pl.pallas_call(
        body, name=name,
        out_shape=tuple(pltpu.HBM(z.shape, z.dtype) for z in zones),
        in_specs=[HBM] * ns + [SEM, SEM, ANY],
        out_specs=(HBM,) * ns,
        input_output_aliases={i: i for i in range(ns)},
        compiler_params=pltpu.CompilerParams(has_side_effects=EFFECT),
    )(*zones, send_sems, recv_sems, after)


def _gather_pass_on(rows, zones, barrier_id, name):
    ns = len(zones)

    def body(*refs):
        _pair_handshake(*_position())
        copies = _pass_on_copies(rows, refs[:ns], refs[ns:2 * ns], *refs[2 * ns:])
        for cp in copies:
            cp.start()
        for cp in copies:
            cp.wait_recv()
        for cp in copies:
            cp.wait_send()

    return pl.pallas_call(
        body, name=name,
        out_shape=tuple(jax.ShapeDtypeStruct(z.shape, z.dtype) for z in zones),
        in_specs=[ANY] * ns, out_specs=(ANY,) * ns,
        input_output_aliases={i: i for i in range(ns)},
        scratch_shapes=[pltpu.SemaphoreType.DMA((3 * ns,)), pltpu.SemaphoreType.DMA((3 * ns,))],
        compiler_params=pltpu.CompilerParams(collective_id=barrier_id),
    )(*zones)


def _pair_copies(sections, g_refs, land, send_sems, recv_sems):
    ns = len(sections)
    x, y, c = _position()
    return [pltpu.make_async_remote_copy(
        src_ref=_block_of(g_refs[s], 2 * k + 1 - c, rows), dst_ref=land[s].at[k],
        send_sem=send_sems.at[k * ns + s], recv_sem=recv_sems.at[k * ns + s],
        device_id=(x, y, 1 - c), device_id_type=MESH_ID)
        for k in range(N_CHIPS) for s, (_, rows, _) in enumerate(sections)]


def _pair_exchange_start(sections, grads, barrier_id, name):
    ns = len(sections)

    def body(*refs):
        _pair_handshake(*_position())
        for cp in _pair_copies(sections, refs[:ns], refs[ns:2 * ns], refs[2 * ns], refs[2 * ns + 1]):
            cp.start()
        refs[-1][...] = jnp.zeros_like(refs[-1])

    zones = [lax.empty((N_CHIPS, rows, cols), BF16) for _, rows, cols in sections]
    n = N_CHIPS * ns
    out = pl.pallas_call(
        body, name=name,
        out_shape=(pltpu.SemaphoreType.DMA((n,)), pltpu.SemaphoreType.DMA((n,)),
                   *[pltpu.HBM(a.shape, a.dtype) for a in (*grads, *zones)],
                   jax.ShapeDtypeStruct((8, LANES), F32)),
        in_specs=[HBM] * (2 * ns),
        out_specs=(SEM, SEM, *[HBM] * (2 * ns), pl.BlockSpec(memory_space=pltpu.VMEM)),
        input_output_aliases={i: 2 + i for i in range(2 * ns)},
        compiler_params=pltpu.CompilerParams(has_side_effects=EFFECT, collective_id=barrier_id),
    )(*[_in_hbm(a) for a in grads], *[_in_hbm(a) for a in zones])
    return out[0], out[1], out[2:2 + ns], out[2 + ns:2 + 2 * ns], out[-1]


def _pair_exchange_wait(sections, send_sems, recv_sems, grads, zones, after, name):
    ns = len(sections)

    def body(*refs):
        for cp in _pair_copies(sections, refs[:ns], refs[ns:2 * ns], refs[2 * ns], refs[2 * ns + 1]):
            cp.wait_send()
            cp.wait_recv()

    out = pl.pallas_call(
        body, name=name,
        out_shape=tuple(pltpu.HBM(a.shape, a.dtype) for a in (*grads, *zones)),
        in_specs=[HBM] * (2 * ns) + [SEM, SEM, ANY],
        out_specs=(HBM,) * (2 * ns),
        input_output_aliases={i: i for i in range(2 * ns)},
        compiler_params=pltpu.CompilerParams(has_side_effects=EFFECT),
    )(*grads, *zones, send_sems, recv_sems, after)
    return out[:ns], out[ns:]


def _pair_add(sections, grads, got, core, name):
    ns = len(sections)

    def body(core_ref, *refs):
        g_refs, got_refs, p_refs = refs[:ns], refs[ns:2 * ns], refs[2 * ns:]
        for s in range(ns):
            p_refs[s][0] = (g_refs[s][...].astype(F32) + got_refs[s][0].astype(F32)).astype(BF16)

    slot = [pl.BlockSpec((1, rows, cols), lambda k, c: (k, 0, 0)) for _, rows, cols in sections]
    return pl.pallas_call(
        body, name=name,
        out_shape=tuple(jax.ShapeDtypeStruct((N_CHIPS, rows, cols), BF16) for _, rows, cols in sections),
        grid_spec=pltpu.PrefetchScalarGridSpec(
            num_scalar_prefetch=1, grid=(N_CHIPS,),
            in_specs=[pl.BlockSpec((rows, cols), lambda k, c: (2 * k + c[0], 0)) for _, rows, cols in sections]
            + slot,
            out_specs=tuple(slot)),
        compiler_params=_params(dimension_semantics=("parallel",)),
    )(core, *grads, *got)


def _chip_copies(sections, p_refs, land, send_sems, recv_sems):
    ns = len(sections)
    x, y, c = _position()
    return [pltpu.make_async_remote_copy(
        src_ref=p_refs[s].at[2 * cx + cy], dst_ref=land[s].at[j],
        send_sem=send_sems.at[j * ns + s], recv_sem=recv_sems.at[j * ns + s],
        device_id=(cx, cy, c), device_id_type=MESH_ID)
        for j, (cx, cy) in enumerate(_other_chips(x, y)) for s in range(ns)]


def _chip_exchange(sections, parts, name):
    ns = len(sections)

    def body(*refs):
        copies = _chip_copies(sections, refs[:ns], refs[ns:2 * ns], *refs[2 * ns:])
        for cp in copies:
            cp.start()
        for cp in copies:
            cp.wait_recv()
        for cp in copies:
            cp.wait_send()

    n = 3 * ns
    return pl.pallas_call(
        body, name=name,
        out_shape=tuple(jax.ShapeDtypeStruct((3, rows, cols), BF16) for _, rows, cols in sections),
        in_specs=[ANY] * ns, out_specs=(ANY,) * ns,
        scratch_shapes=[pltpu.SemaphoreType.DMA((n,)), pltpu.SemaphoreType.DMA((n,))],
    )(*parts)


def _chip_exchange_start(sections, parts, name):
    ns = len(sections)

    def body(*refs):
        p_refs, land = refs[:ns], refs[ns:2 * ns]
        send_sems, recv_sems = refs[2 * ns], refs[2 * ns + 1]
        token = refs[-1]
        for cp in _chip_copies(sections, p_refs, land, send_sems, recv_sems):
            cp.start()
        token[...] = jnp.zeros_like(token)

    zones = [lax.empty((3, rows, cols), BF16) for _, rows, cols in sections]
    out = pl.pallas_call(
        body, name=name,
        out_shape=(pltpu.SemaphoreType.DMA((3 * ns,)), pltpu.SemaphoreType.DMA((3 * ns,)),
                   *[pltpu.HBM(a.shape, a.dtype) for a in parts], *[pltpu.HBM(a.shape, a.dtype) for a in zones],
                   jax.ShapeDtypeStruct((8, LANES), F32)),
        in_specs=[HBM] * (2 * ns),
        out_specs=(SEM, SEM, *[HBM] * (2 * ns), pl.BlockSpec(memory_space=pltpu.VMEM)),
        input_output_aliases={i: 2 + i for i in range(2 * ns)},
        compiler_params=pltpu.CompilerParams(has_side_effects=EFFECT),
    )(*[_in_hbm(a) for a in parts], *[_in_hbm(a) for a in zones])
    return out[0], out[1], out[2:2 + ns], out[2 + ns:2 + 2 * ns], out[-1]


def _chip_exchange_wait(sections, send_sems, recv_sems, parts, zones, after, name):
    ns = len(sections)

    def body(*refs):
        p_refs, land = refs[:ns], refs[ns:2 * ns]
        for cp in _chip_copies(sections, p_refs, land, refs[2 * ns], refs[2 * ns + 1]):
            cp.wait_send()
            cp.wait_recv()

    out = pl.pallas_call(
        body, name=name,
        out_shape=tuple(pltpu.HBM(a.shape, a.dtype) for a in (*parts, *zones)),
        in_specs=[HBM] * (2 * ns) + [SEM, SEM, ANY],
        out_specs=(HBM,) * (2 * ns),
        input_output_aliases={i: i for i in range(2 * ns)},
        compiler_params=pltpu.CompilerParams(has_side_effects=EFFECT),
    )(*parts, *zones, send_sems, recv_sems, after)
    return out[:ns], out[ns:]


def _grad_finish(sections, parts, far, chip, name):
    ns = len(sections)

    def body(chip_ref, *refs):
        p_refs, b_refs, g_refs = refs[:ns], refs[ns:2 * ns], refs[2 * ns:]
        for s in range(ns):
            g = p_refs[s][0].astype(F32)
            for j in range(3):
                g = g + b_refs[s][j].astype(F32)
            g_refs[s][...] = g

    half = [(rows // 2, cols) for _, rows, cols in sections]
    return pl.pallas_call(
        body, name=name,
        out_shape=tuple(jax.ShapeDtypeStruct((rows, cols), F32) for _, rows, cols in sections),
        grid_spec=pltpu.PrefetchScalarGridSpec(
            num_scalar_prefetch=1, grid=(2,),
            in_specs=[pl.BlockSpec((1, r, c), lambda i, chip: (chip[0], i, 0)) for r, c in half]
            + [pl.BlockSpec((3, r, c), lambda i, chip: (0, i, 0)) for r, c in half],
            out_specs=tuple(pl.BlockSpec((r, c), lambda i, chip: (i, 0)) for r, c in half)),
        compiler_params=_params(dimension_semantics=("parallel",)),
    )(chip, *parts, *far)


def _sum_devices(parts, rows, name):
    cols = parts.shape[1]
    tr = rows // 2

    def body(*refs):
        s = refs[0][...].astype(F32)
        for d in range(1, N_DEV):
            s = s + refs[d][...].astype(F32)
        refs[N_DEV][...] = s

    return pl.pallas_call(
        body, name=name,
        out_shape=jax.ShapeDtypeStruct((rows, cols), F32),
        grid=(2,),
        in_specs=[pl.BlockSpec((tr, cols), lambda i, d=d: (2 * d + i, 0)) for d in range(N_DEV)],
        out_specs=pl.BlockSpec((tr, cols), lambda i: (i, 0)),
        compiler_params=_params(dimension_semantics=("parallel",)),
    )(*([parts] * N_DEV))


def _adamw_step(w_ref, g_ref, m_ref, v_ref, d_ref, nm_ref, nv_ref):
    c1 = 1.0 / (1.0 - ADAM_B1 ** ADAM_STEP)
    c2 = 1.0 / (1.0 - ADAM_B2 ** ADAM_STEP)
    gv = g_ref[...]
    nm = ADAM_B1 * m_ref[...] + (1.0 - ADAM_B1) * gv
    nv = ADAM_B2 * v_ref[...] + (1.0 - ADAM_B2) * (gv * gv)
    nm_ref[...] = nm
    nv_ref[...] = nv
    d_ref[...] = (-ADAM_LR) * ((nm * c1) / (jnp.sqrt(nv * c2) + ADAM_EPS) + ADAM_WD * w_ref[...])


def _adamw_small(params, name):
    n = len(params)

    def body(*refs):
        for k in range(n):
            _adamw_step(*refs[4 * k:4 * k + 4], *refs[4 * n + 3 * k:4 * n + 3 * k + 3])

    out = pl.pallas_call(
        body, name=name,
        out_shape=tuple(jax.ShapeDtypeStruct(p[0].shape, F32) for p in params for _ in range(3)),
    )(*[a for p in params for a in p])
    return [out[3 * k:3 * k + 3] for k in range(n)]


def _adamw(w, g, m, v, name, after=None):
    rows, cols = w.shape
    tr = rows
    while tr * cols * 4 > (1 << 20) and tr % 16 == 0:
        tr //= 2
    tokens = [] if after is None else [after]

    def body(*refs):
        _adamw_step(*refs[:4], *refs[4 + len(tokens):])

    spec = pl.BlockSpec((tr, cols), lambda i: (i, 0))
    shape = jax.ShapeDtypeStruct((rows, cols), F32)
    return pl.pallas_call(
        body, name=name,
        out_shape=(shape, shape, shape),
        grid=(rows // tr,),
        in_specs=[spec] * 4 + [ANY] * len(tokens), out_specs=(spec,) * 3,
        compiler_params=_params(dimension_semantics=("parallel",)),
    )(w, g, m, v, *tokens)


NAMES = ("ln1_g", "w_in", "b_in", "rpb", "w_att_o", "conv_w", "conv_b", "w_rg_a", "b_rg_a", "w_rg_i",
         "b_rg_i", "lru_lambda", "w_rec_o", "w_out", "ln2_g", "w_ff1", "w_ff2", "lnf_g")
TRANSPOSED = {"w_in": "w_in_t", "w_att_o": "w_att_o_t", "w_ff1": "w_ff1_t"}
ROW_SHARDED = ("w_rec_o", "w_out", "w_ff2")
REPLICATED = (("ln1_g", (1, D)), ("b_in", (1, D_IN)), ("rpb", (N_HEADS * N_RPB_R, N_RPB_C)),
              ("conv_b", (1, D_REC)), ("w_rg_a", (2 * N_REC_BLOCKS * REC_BLOCK, REC_BLOCK)),
              ("w_rg_i", (2 * N_REC_BLOCKS * REC_BLOCK, REC_BLOCK)), ("ln2_g", (1, D)), ("lnf_g", (1, D)))
GATE_BLOCKS = ("w_rg_a", "w_rg_i")
SMALL_ROWS = 112


def _chan_bits(vectors):
    chan = jnp.concatenate(vectors, axis=0)
    bits = lax.bitcast_convert_type(chan, BF16).reshape(-1)
    return jnp.pad(bits, (0, CHAN_BLOCK_ROWS * D - bits.shape[0])).reshape(CHAN_BLOCK_ROWS, D)


def _chan_from_bits(gathered):
    bits = gathered.reshape(N_DEV, CHAN_BLOCK_ROWS * D)[:, :2 * N_CHAN_ROWS * LANES]
    chan = lax.bitcast_convert_type(bits.reshape(N_DEV, N_CHAN_ROWS, LANES, 2), F32)
    return chan.transpose(1, 0, 2).reshape(N_CHAN_ROWS, D)


def kernel(x, ln1_g, w_in, b_in, rpb, w_att_o, conv_w, conv_b, w_rg_a, b_rg_a, w_rg_i, b_rg_i, lru_lambda, w_rec_o, w_out, ln2_g, w_ff1, w_ff2, lnf_g, loss_target, m_ln1_g, m_w_in, m_b_in, m_rpb, m_w_att_o, m_conv_w, m_conv_b, m_w_rg_a, m_b_rg_a, m_w_rg_i, m_b_rg_i, m_lru_lambda, m_w_rec_o, m_w_out, m_ln2_g, m_w_ff1, m_w_ff2, m_lnf_g, v_ln1_g, v_w_in, v_b_in, v_rpb, v_w_att_o, v_conv_w, v_conv_b, v_w_rg_a, v_b_rg_a, v_w_rg_i, v_b_rg_i, v_lru_lambda, v_w_rec_o, v_w_out, v_ln2_g, v_w_ff1, v_w_ff2, v_lnf_g):
    w = dict(zip(NAMES, (ln1_g, w_in, b_in, rpb, w_att_o, conv_w, conv_b, w_rg_a, b_rg_a, w_rg_i,
                         b_rg_i, lru_lambda, w_rec_o, w_out, ln2_g, w_ff1, w_ff2, lnf_g)))
    m = dict(zip(NAMES, (m_ln1_g, m_w_in, m_b_in, m_rpb, m_w_att_o, m_conv_w, m_conv_b, m_w_rg_a,
                         m_b_rg_a, m_w_rg_i, m_b_rg_i, m_lru_lambda, m_w_rec_o, m_w_out, m_ln2_g,
                         m_w_ff1, m_w_ff2, m_lnf_g)))
    v = dict(zip(NAMES, (v_ln1_g, v_w_in, v_b_in, v_rpb, v_w_att_o, v_conv_w, v_conv_b, v_w_rg_a,
                         v_b_rg_a, v_w_rg_i, v_b_rg_i, v_lru_lambda, v_w_rec_o, v_w_out, v_ln2_g,
                         v_w_ff1, v_w_ff2, v_lnf_g)))
    xi, yi, ci = _position()

    shard = {t: w[n][0].T.astype(BF16) for n, t in TRANSPOSED.items()}
    shard.update({n: w[n][0].astype(BF16) for n in ROW_SHARDED})
    shard["chan"] = _chan_bits([w[n][0] for n, _ in CHAN])
    first, later = ("w_in_t", "chan"), ("w_rec_o", "w_out", "w_att_o_t", "w_ff1_t", "w_ff2")
    p = dict(zip(first, _all_gather([shard[n] for n in first], "weight_all_gather")))
    send_sems, recv_sems, sent, zones, token = _gather_start([shard[n] for n in later], p["w_in_t"],
                                                             "weight_gather_start")

    stages = (("w_rec_o", "w_out", "w_att_o_t"), ("w_ff1_t", "w_ff2"))
    passing = {}

    def late_weights(after, stage):
        which = [later.index(n) for n in stages[stage]]
        _, arrived = _gather_wait(
            send_sems, recv_sems, [shard[n].shape[0] for n in later], which, [sent[i] for i in which],
            [zones[i] for i in which], after, "weight_gather_wait_%d" % stage)
        passing[stage] = _pass_on_start(
            [shard[n].shape[0] for n in stages[stage]], arrived,
            PASS_ON_IDS[stage], "weight_pass_on_start_%d" % stage)
        return passing[stage][-1]

    def late_weights_ready(after, stage):
        pass_send_sems, pass_recv_sems, pass_zones, _ = passing[stage]
        return dict(zip(stages[stage], _pass_on_wait(
            [shard[n].shape[0] for n in stages[stage]], pass_send_sems, pass_recv_sems, pass_zones, after,
            "weight_pass_on_wait_%d" % stage)))

    chan = _chan_from_bits(p.pop("chan"))
    r0 = 0
    for n, rows in CHAN:
        p[n] = chan[r0:r0 + rows]
        r0 += rows
    p.update(ln1_g=w["ln1_g"], b_in=w["b_in"], later_weights_started=token, rpb=w["rpb"][0], conv_b=w["conv_b"],
             w_rg_a=w["w_rg_a"][0], w_rg_i=w["w_rg_i"][0], ln2_g=w["ln2_g"],
             lnf_g=w["lnf_g"].reshape(1, D))

    core = jnp.reshape(ci, (1,)).astype(jnp.int32)
    chip = jnp.reshape(2 * xi + yi, (1,)).astype(jnp.int32)
    first_sections = tuple(s for s in SECTIONS if s[0] in ("w_ff1_t", "w_ff2"))
    late_sections = SECTIONS[:1]
    early_sections = tuple(s for s in SECTIONS[1:] if s not in first_sections)
    in_flight = {}

    def pair_sum_and_send(group, sections, after):
        send_sems, recv_sems, sect, zones, _ = in_flight["pair_" + group]
        sect, got = _pair_exchange_wait(sections, send_sems, recv_sems, sect, zones, after,
                                        "grad_pair_exchange_wait_" + group)
        parts = _pair_add(sections, sect, got, core, "grad_pair_add_" + group)
        in_flight[group] = _chip_exchange_start(sections, parts, "grad_chip_exchange_start_" + group)
        return in_flight[group][-1]

    def pair_exchange_at_once(group, sections, grads, barrier_id):
        in_flight["pair_" + group] = _pair_exchange_start(
            sections, [grads[n] for n, _, _ in sections], barrier_id, "grad_pair_exchange_start_" + group)
        return pair_sum_and_send(group, sections, in_flight["pair_" + group][-1])

    def reduce_first(grads, after):
        if grads is None:
            return pair_sum_and_send("first", first_sections, after)
        in_flight["pair_first"] = _pair_exchange_start(
            first_sections, [grads[n] for n, _, _ in first_sections], PAIR_FIRST_ID,
            "grad_pair_exchange_start_first")
        return in_flight["pair_first"][-1]

    def reduce_early(grads):
        chan_g = jnp.concatenate([grads[n] for n, _ in CHAN], axis=0)
        chan_g = chan_g.reshape(N_CHAN_ROWS, N_DEV, LANES).transpose(1, 0, 2).astype(BF16)
        chan_g = jnp.pad(chan_g.reshape(N_DEV, -1), ((0, 0), (0, CHAN_BLOCK_ROWS * D - N_CHAN_ROWS * LANES)))
        grads["chan"] = chan_g.reshape(N_DEV * CHAN_BLOCK_ROWS, D)
        grads["gates"] = jnp.concatenate([grads[n].reshape(-1, D) for n in GATE_BLOCKS], axis=0).astype(BF16)
        return pair_exchange_at_once("early", early_sections, grads, PAIR_EARLY_ID)

    def finish(group, sections, after, name):
        send_sems, recv_sems, parts, zones, _ = in_flight[group]
        parts, far = _chip_exchange_wait(sections, send_sems, recv_sems, parts, zones, after,
                                         "grad_chip_exchange_wait_" + name)
        return dict(zip((n for n, _, _ in sections),
                        _grad_finish(sections, parts, far, chip, "grad_finish_" + name)))

    summed = {}

    def reduce_late(grads):
        in_flight["pair_late"] = _pair_exchange_start(
            late_sections, [grads[n] for n, _, _ in late_sections], PAIR_LATE_ID,
            "grad_pair_exchange_start_late")
        summed.update(finish("first", first_sections, in_flight["pair_late"][-1], "first"))
        summed.update(finish("early", early_sections, summed["w_ff2"], "early"))
        return pair_sum_and_send("late", late_sections, summed["gates"])

    loss_part, grad_x, grads = _local_step(x[0], loss_target[0], p, late_weights, late_weights_ready,
                                           reduce_first, reduce_early, reduce_late)

    flat = jnp.concatenate([grads[n].reshape(-1) for n, _ in REPLICATED if n not in GATE_BLOCKS]
                           + [loss_part.reshape(-1)])
    n_small = flat.shape[0]
    flat = jnp.pad(flat, (0, SMALL_ROWS * LANES - n_small)).reshape(SMALL_ROWS, LANES)
    *small_gather, small_started = _gather_start([flat, summed["gates"]], None, "small_grad_gather_start")

    g, delta, new_m, new_v = {}, {}, {}, {}

    def update(n, g2, shape2, after=None):
        d2, m2, v2 = _adamw(w[n].reshape(shape2), g2, m[n].reshape(shape2), v[n].reshape(shape2),
                            "adamw_" + n, after)
        g[n], delta[n], new_m[n], new_v[n] = (a.reshape(w[n].shape) for a in (g2, d2, m2, v2))

    for n in ROW_SHARDED:
        update(n, summed[n], summed[n].shape, small_started)
    for n, t in TRANSPOSED.items():
        if t in summed:
            update(n, summed[t].T, summed[t].shape[::-1], small_started)

    small_rows = [SMALL_ROWS, GATE_ROWS]
    _, small_zones = _gather_wait(
        *small_gather[:2], small_rows, range(2), *small_gather[2:],
        _after_all(list(delta.values()), "sharded_updates_done"), "small_grad_gather_wait")
    small_parts, gate_sum = _gather_pass_on(small_rows, small_zones, SMALL_PASS_ON_ID,
                                            "small_grad_gather_pass_on")
    small = _sum_devices(small_parts, SMALL_ROWS, "small_grad_sum").reshape(-1)
    loss = small[n_small - 1]

    small_params = []
    o = 0
    for n, shape2 in REPLICATED:
        if n in GATE_BLOCKS:
            k, rows = GATE_BLOCKS.index(n), gate_sum.shape[0] // len(GATE_BLOCKS)
            update(n, gate_sum[k * rows:(k + 1) * rows].reshape(shape2), shape2)
        else:
            size = shape2[0] * shape2[1]
            small_params.append((n, small[o:o + size].reshape(shape2), shape2))
            o += size
    chan_back = summed["chan"].reshape(-1)[:N_CHAN_ROWS * LANES].reshape(N_CHAN_ROWS, LANES)
    r0 = 0
    for n, rows in CHAN:
        small_params.append((n, chan_back[r0:r0 + rows], (rows, LANES)))
        r0 += rows
    results = _adamw_small([(w[n].reshape(s2), g2, m[n].reshape(s2), v[n].reshape(s2))
                            for n, g2, s2 in small_params], "adamw_vectors")
    for (n, g2, _), (d2, m2, v2) in zip(small_params, results):
        g[n], delta[n], new_m[n], new_v[n] = (a.reshape(w[n].shape) for a in (g2, d2, m2, v2))

    summed = finish("late", late_sections, _after_all(list(delta.values()), "updates_done"), "late")
    g_t = summed["w_in_t"]
    results = _adamw(w["w_in"][0].T, g_t, m["w_in"][0].T, v["w_in"][0].T, "adamw_w_in")
    g["w_in"], delta["w_in"], new_m["w_in"], new_v["w_in"] = (a.T[None] for a in (g_t, *results))

    return (loss, grad_x[None], *[g[n] for n in NAMES], *[delta[n] for n in NAMES],
            *[new_m[n] for n in NAMES], *[new_v[n] for n in NAMES])
```

```python
import math

import numpy as np
import jax
import jax.numpy as jnp
from jax import lax
from jax.experimental import pallas as pl
from jax.experimental.pallas import tpu as pltpu

F32 = jnp.float32
BF16 = jnp.bfloat16

T = 2048
D = 1024
D_ATT = 512
D_REC = 1024
D_FF = 4096
D_IN = 5632
N_HEADS = 8
DH = 64
GRID_W = 64
ROWS = T // GRID_W
WIN_H = 8
WIN_W = 16
KWIN = WIN_H * GRID_W
N_RPB_R = 2 * WIN_H - 1
N_RPB_C = 2 * WIN_W - 1
N_REC_BLOCKS = 16
REC_BLOCK = 64
CG = 128
N_CG = D_REC // CG
LRU_C = 8.0
EPS = 1e-6
N_DEV = 8
N_CHIPS = 4
LANES = 128

ADAM_LR = 0.001
ADAM_B1 = 0.9
ADAM_B2 = 0.999
ADAM_EPS = 1e-08
ADAM_WD = 0.01
ADAM_STEP = 10

MESH_AXES = ("x", "y", "c")
VMEM_LIMIT = 56 * 1024 * 1024

TILE = 512
DZ_ARRAYS = ((0, 3, 1), (3, 4, 2), (7, 4, 2))
N_DZ_TILES = D_IN // TILE


def _params(**kw):
    return pltpu.CompilerParams(vmem_limit_bytes=VMEM_LIMIT, **kw)


HG = 4
HQ = HG * GRID_W
HC = HG * DH


def _att_tables():
    rq = np.arange(GRID_W)
    kc = np.arange(KWIN) % GRID_W
    win_start = np.clip(rq - WIN_W // 2, 0, GRID_W - WIN_W)
    valid = (kc[None, :] >= win_start[:, None]) & (kc[None, :] < win_start[:, None] + WIN_W)
    same_head = (np.arange(HQ)[:, None] // GRID_W) == (np.arange(HC)[None, :] // DH)
    return valid.astype(np.float32), same_head.astype(np.float32)


def _pair_mask():
    half = np.arange(2 * DH) // DH
    return (half[:, None] == half[None, :]).astype(np.float32)


def _dup_table():
    return np.concatenate([np.eye(REC_BLOCK, dtype=np.float32)] * 2, axis=1)


def _sigmoid(x):
    return 0.5 * jnp.tanh(0.5 * x) + 0.5


def _softplus(x):
    return jnp.maximum(x, 0.0) + jnp.log(1.0 + jnp.exp(-jnp.abs(x)))


def _one_minus_square(log_a, a):
    x = 2.0 * log_a
    series = -x * (1.0 + x * (0.5 + x * (1.0 / 6.0)))
    return jnp.where(x > -0.02, series, 1.0 - a * a)


_GELU_C = math.sqrt(2.0 / math.pi)


def _gelu_and_grad(x):
    x2 = x * x
    inner = _GELU_C * (x + 0.044715 * x * x2)
    t = jnp.tanh(inner)
    g = 0.5 * x * (1.0 + t)
    dg = 0.5 * (1.0 + t) + 0.5 * x * (1.0 - t * t) * _GELU_C * (1.0 + 3.0 * 0.044715 * x2)
    return g, dg


def _dot(a, b):
    return jnp.dot(a, b, preferred_element_type=F32)


def _dot_nt(a, b):
    return lax.dot_general(a, b, (((1,), (1,)), ((), ())), preferred_element_type=F32)


def _dot_tn(a, b):
    return lax.dot_general(a, b, (((0,), (0,)), ((), ())), preferred_element_type=F32)


def _dot_exact(a, b):
    return jnp.dot(a, b, precision=lax.Precision.HIGHEST, preferred_element_type=F32)


def _shift_rows(x, s):
    n = x.shape[0]
    rows = lax.broadcasted_iota(jnp.int32, x.shape, 0)
    y = pltpu.roll(x, s % n, 0)
    if s > 0:
        return jnp.where(rows >= s, y, 0.0)
    return jnp.where(rows < n + s, y, 0.0)


def _rms_bwd(dh, xh, r, g):
    dxh = dh * g
    return r * (dxh - xh * jnp.mean(dxh * xh, axis=-1, keepdims=True))


def _matmul(a, b, mode, out_dtype, name, tm=512, tn=1024, tk=2048):
    if mode == "nn":
        (m, k), (k2, n) = a.shape, b.shape
    elif mode == "nt":
        (m, k), (n, k2) = a.shape, b.shape
    else:
        (k, m), (k2, n) = a.shape, b.shape
    assert k == k2
    tm, tn, tk = min(tm, m), min(tn, n), min(tk, k)
    assert m % tm == 0 and n % tn == 0 and k % tk == 0
    nk = k // tk
    dot = {"nn": _dot, "nt": _dot_nt, "tn": _dot_tn}[mode]

    def body(a_ref, b_ref, o_ref, acc):
        kk = pl.program_id(2)
        part = dot(a_ref[...].astype(BF16), b_ref[...].astype(BF16))
        if nk == 1:
            o_ref[...] = part.astype(out_dtype)
            return

        @pl.when(kk == 0)
        def _():
            acc[...] = part

        @pl.when(kk > 0)
        def _():
            acc[...] += part

        @pl.when(kk == nk - 1)
        def _():
            o_ref[...] = acc[...].astype(out_dtype)

    if mode == "tn":
        a_spec = pl.BlockSpec((tk, tm), lambda i, j, kk: (kk, i))
    else:
        a_spec = pl.BlockSpec((tm, tk), lambda i, j, kk: (i, kk))
    if mode == "nt":
        b_spec = pl.BlockSpec((tn, tk), lambda i, j, kk: (j, kk))
    else:
        b_spec = pl.BlockSpec((tk, tn), lambda i, j, kk: (kk, j))
    return pl.pallas_call(
        body, name=name,
        out_shape=jax.ShapeDtypeStruct((m, n), out_dtype),
        grid=(m // tm, n // tn, nk),
        in_specs=[a_spec, b_spec],
        out_specs=pl.BlockSpec((tm, tn), lambda i, j, kk: (i, j)),
        scratch_shapes=[pltpu.VMEM((tm, tn) if nk > 1 else (8, LANES), F32)],
        compiler_params=_params(dimension_semantics=("parallel", "parallel", "arbitrary")),
    )(a, b)


def _in_proj(x, g1, w_in_t, b_in, after):
    tm = 512

    def body(x_ref, g_ref, w_hbm, b_ref, after_ref, qkv_ref, uy_ref, gg_ref, h_ref, w):
        @pl.when(pl.program_id(0) == 0)
        def _():
            pltpu.sync_copy(w_hbm, w)

        xv = x_ref[...]
        r = lax.rsqrt(jnp.mean(xv * xv, axis=-1, keepdims=True) + EPS)
        h = ((xv * r) * g_ref[...]).astype(BF16)
        h_ref[...] = h
        row0 = 0
        for ref in (qkv_ref, uy_ref, gg_ref):
            for c0 in range(0, ref.shape[1], TILE):
                z = _dot_nt(h, w[row0:row0 + TILE, :]) + b_ref[:, row0:row0 + TILE]
                ref[:, c0:c0 + TILE] = z.astype(ref.dtype)
                row0 += TILE

    tok = lambda width: pl.BlockSpec((tm, width), lambda i: (i, 0))
    return pl.pallas_call(
        body, name="in_proj",
        out_shape=(jax.ShapeDtypeStruct((T, 3 * D_ATT), BF16),
                   jax.ShapeDtypeStruct((T, 2 * D_REC), F32),
                   jax.ShapeDtypeStruct((T, 2 * D), F32),
                   jax.ShapeDtypeStruct((T, D), BF16)),
        grid=(T // tm,),
        in_specs=[tok(D), pl.BlockSpec((1, D), lambda i: (0, 0)), pl.BlockSpec(memory_space=pl.ANY),
                  pl.BlockSpec((1, D_IN), lambda i: (0, 0)), pl.BlockSpec(memory_space=pl.ANY)],
        out_specs=(tok(3 * D_ATT), tok(2 * D_REC), tok(2 * D), tok(D)),
        scratch_shapes=[pltpu.VMEM((D_IN, D), BF16)],
        compiler_params=_params(dimension_semantics=("arbitrary",)),
    )(x, g1, w_in_t, b_in, after)


def _dz_specs(rows, tile_of, row_of):
    def spec(off, n, per_plane):
        def index(*ids):
            t = jnp.clip(tile_of(*ids) - off, 0, n - 1)
            return (t // per_plane, row_of(*ids), t % per_plane)
        return pl.BlockSpec((1, rows, TILE), index)
    return [spec(off, n, per) for off, n, per in DZ_ARRAYS]


def _dh_norm1_bwd(dz, w_in_t, x, g1, dx1, after):
    tm = 512

    def body(dqkv_ref, duy_ref, dgg_ref, w_hbm, x_ref, g_ref, dx1_ref, after_ref, gx_ref, dg_ref, w):
        @pl.when(pl.program_id(0) == 0)
        def _():
            pltpu.sync_copy(w_hbm, w)
            dg_ref[...] = jnp.zeros_like(dg_ref)

        dh, row0 = None, 0
        for ref in (dqkv_ref, duy_ref, dgg_ref):
            for plane in range(ref.shape[0]):
                cols = ref.shape[2]
                part = _dot(ref[plane], w[row0:row0 + cols, :])
                dh = part if dh is None else dh + part
                row0 += cols
        xv = x_ref[...]
        r = lax.rsqrt(jnp.mean(xv * xv, axis=-1, keepdims=True) + EPS)
        xh = xv * r
        dg_ref[...] += jnp.sum(dh * xh, axis=0, keepdims=True)
        gx_ref[...] = dx1_ref[...] + _rms_bwd(dh, xh, r, g_ref[...])

    tok = pl.BlockSpec((tm, D), lambda i: (i, 0))
    vec = pl.BlockSpec((1, D), lambda i: (0, 0))
    planes = lambda a: pl.BlockSpec((a.shape[0], tm, a.shape[2]), lambda i: (0, i, 0))
    return pl.pallas_call(
        body, name="dh_norm1_bwd",
        out_shape=(jax.ShapeDtypeStruct((T, D), F32), jax.ShapeDtypeStruct((1, D), F32)),
        grid=(T // tm,),
        in_specs=[planes(a) for a in dz] + [pl.BlockSpec(memory_space=pl.ANY), tok, vec, tok,
                                            pl.BlockSpec(memory_space=pl.ANY)],
        out_specs=(tok, vec),
        scratch_shapes=[pltpu.VMEM((D_IN, D), BF16)],
        compiler_params=_params(dimension_semantics=("arbitrary",)),
    )(*dz, w_in_t, x, g1, dx1, after)


def _grad_w_in(dz, h):
    def body(*refs):
        seg_refs = refs[:3]
        h_ref, gw_ref, gb_ref = refs[3:]
        j = pl.program_id(0)

        for s, (off, n, _) in enumerate(DZ_ARRAYS):
            @pl.when((j >= off) & (j < off + n))
            def _(s=s):
                a = seg_refs[s][0]
                gw_ref[...] = _dot_tn(a, h_ref[...]).astype(BF16)
                gb_ref[...] = jnp.sum(a.astype(F32), axis=0, keepdims=True)

    return pl.pallas_call(
        body, name="grad_w_in",
        out_shape=(jax.ShapeDtypeStruct((D_IN, D), BF16), jax.ShapeDtypeStruct((1, D_IN), F32)),
        grid=(N_DZ_TILES,),
        in_specs=_dz_specs(T, lambda j: j, lambda j: 0) + [pl.BlockSpec((T, D), lambda j: (0, 0))],
        out_specs=(pl.BlockSpec((TILE, D), lambda j: (j, 0)), pl.BlockSpec((1, TILE), lambda j: (0, j))),
        compiler_params=_params(dimension_semantics=("parallel",)),
    )(*dz, h)


def _rpb_rows(rpb):
    padded = jnp.pad(rpb, ((0, 0), (0, 0), (0, GRID_W - N_RPB_C)))
    rows = [padded[:, WIN_H - 1 - oi: 2 * WIN_H - 1 - oi].reshape(N_HEADS // HG, HG, KWIN)
            for oi in range(WIN_H)]
    return jnp.stack(rows, axis=0)


SKEW = KWIN - (WIN_W - 1)


MASKED = -1e30


def _bias_tiles(rows_ref, valid, bias_s):
    for oi in range(WIN_H):
        for hh in range(HG):
            row = jnp.broadcast_to(rows_ref[oi, 0, hh:hh + 1, :], (GRID_W, KWIN))
            tile = pltpu.roll(row, SKEW, 1, stride=1, stride_axis=0)
            bias_s[oi, hh * GRID_W:(hh + 1) * GRID_W, :] = jnp.where(valid, tile, MASKED)


def _bias_tile_grads(gb_s, flip, out_ref):
    for oi in range(WIN_H):
        for hh in range(HG):
            g = _dot_exact(flip, gb_s[oi, hh * GRID_W:(hh + 1) * GRID_W, :])
            back = pltpu.roll(g, KWIN - (GRID_W - WIN_W), 1, stride=1, stride_axis=0)
            out_ref[0, oi, hh:hh + 1, :] = jnp.sum(back, axis=0, keepdims=True)


def _rpb_fold(row_grads):
    g = row_grads.transpose(1, 0, 2, 3).reshape(WIN_H, N_HEADS, WIN_H, GRID_W)
    g = g.transpose(0, 2, 1, 3)

    def body(g_ref, o_ref):
        for dr in range(N_RPB_R):
            terms = [g_ref[oi, i] for oi in range(WIN_H) for i in range(WIN_H) if i - oi + WIN_H - 1 == dr]
            acc = terms[0]
            for term in terms[1:]:
                acc = acc + term
            o_ref[dr] = acc

    out = pl.pallas_call(
        body, name="rpb_fold",
        out_shape=jax.ShapeDtypeStruct((N_RPB_R, N_HEADS, GRID_W), F32),
    )(g)
    return out.transpose(1, 0, 2)[:, :, :N_RPB_C]


ATT_GROUPS = N_HEADS // HG
ATT_UNROLL = 8


def _stacked(rows64, same_head):
    return jnp.where(same_head, jnp.concatenate([rows64] * HG, axis=0), jnp.zeros((), BF16))


def _own_heads(stacked):
    head = lax.broadcasted_iota(jnp.int32, (GRID_W, HC), 1) // DH
    out = stacked[:GRID_W]
    for h in range(1, HG):
        out = jnp.where(head == h, stacked[h * GRID_W:(h + 1) * GRID_W], out)
    return out


def _att_scores(q_ref, k_ref, bias_ref, same_head, r):
    rs = jnp.clip(r - WIN_H // 2, 0, ROWS - WIN_H)
    oi = r - rs
    q0 = pl.multiple_of(r * GRID_W, GRID_W)
    k0 = pl.multiple_of(rs * GRID_W, GRID_W)
    q2 = _stacked(q_ref[pl.ds(q0, GRID_W), :] * (DH ** -0.5), same_head)
    kw = k_ref[pl.ds(k0, KWIN), :]
    s = _dot_nt(q2, kw) + bias_ref[oi]
    e = jnp.exp(s - jnp.max(s, axis=-1, keepdims=True))
    return e, 1.0 / jnp.sum(e, axis=-1, keepdims=True), q2, kw, q0, k0, oi


def _att_specs():
    col = lambda off: pl.BlockSpec((T, HC), lambda g: (0, g + off * ATT_GROUPS))
    tables = [pl.BlockSpec((WIN_H, 1, HG, KWIN), lambda g: (0, g, 0, 0)),
              pl.BlockSpec((GRID_W, KWIN), lambda g: (0, 0)),
              pl.BlockSpec((HQ, HC), lambda g: (0, 0))]
    return col, tables, pltpu.VMEM((WIN_H, HQ, KWIN), F32)


def _att_fwd(qkv, bias_rows, after):
    valid_np, same_head_np = _att_tables()

    def body(q_ref, k_ref, v_ref, rows_ref, valid_ref, head_ref, after_ref, o_ref, bias_s):
        same_head = head_ref[...] > 0.5
        _bias_tiles(rows_ref, valid_ref[...] > 0.5, bias_s)

        def row(r, carry):
            e, rl, _, _, q0, k0, _ = _att_scores(q_ref, k_ref, bias_s, same_head, r)
            o2 = _dot((e * rl).astype(BF16), v_ref[pl.ds(k0, KWIN), :])
            o_ref[pl.ds(q0, GRID_W), :] = _own_heads(o2).astype(BF16)
            return carry

        lax.fori_loop(0, ROWS, row, 0, unroll=ATT_UNROLL)

    col, tables, tiles = _att_specs()
    return pl.pallas_call(
        body, name="att_fwd",
        out_shape=jax.ShapeDtypeStruct((T, D_ATT), BF16),
        grid=(ATT_GROUPS,),
        in_specs=[col(0), col(1), col(2)] + tables + [pl.BlockSpec(memory_space=pl.ANY)],
        out_specs=col(0),
        scratch_shapes=[tiles],
        compiler_params=_params(dimension_semantics=("parallel",)),
    )(qkv, qkv, qkv, bias_rows, jnp.asarray(valid_np), jnp.asarray(same_head_np), after)


def _att_bwd(qkv, bias_rows, datt, after):
    valid_np, same_head_np = _att_tables()

    def body(q_ref, k_ref, v_ref, do_ref, rows_ref, valid_ref, head_ref, flip_ref, after_ref,
             dqkv_ref, grows_ref, dk_acc, dv_acc, bias_s, gb_s):
        same_head = head_ref[...] > 0.5
        dk_acc[...] = jnp.zeros_like(dk_acc)
        dv_acc[...] = jnp.zeros_like(dv_acc)
        gb_s[...] = jnp.zeros_like(gb_s)
        _bias_tiles(rows_ref, valid_ref[...] > 0.5, bias_s)

        def row(r, carry):
            e, rl, q2, kw, q0, k0, oi = _att_scores(q_ref, k_ref, bias_s, same_head, r)
            do2 = _stacked(do_ref[pl.ds(q0, GRID_W), :], same_head)
            vw = v_ref[pl.ds(k0, KWIN), :]
            p = e * rl
            dp = _dot_nt(do2, vw)
            ds = p * (dp - jnp.sum(dp * p, axis=-1, keepdims=True))
            p16 = p.astype(BF16)
            ds16 = ds.astype(BF16)
            dv_acc[pl.ds(k0, KWIN), :] += _dot_tn(p16, do2)
            dk_acc[pl.ds(k0, KWIN), :] += _dot_tn(ds16, q2)
            dq2 = _dot(ds16, kw) * (DH ** -0.5)
            dqkv_ref[0, pl.ds(q0, GRID_W), :] = _own_heads(dq2).astype(BF16)
            gb_s[oi] += ds
            return carry

        lax.fori_loop(0, ROWS, row, 0, unroll=ATT_UNROLL)
        dqkv_ref[1] = dk_acc[...].astype(BF16)
        dqkv_ref[2] = dv_acc[...].astype(BF16)
        _bias_tile_grads(gb_s, flip_ref[...], grows_ref)

    col, tables, tiles = _att_specs()
    return pl.pallas_call(
        body, name="att_bwd",
        out_shape=(jax.ShapeDtypeStruct((3, T, D_ATT), BF16),
                   jax.ShapeDtypeStruct((ATT_GROUPS, WIN_H, HG, KWIN), F32)),
        grid=(ATT_GROUPS,),
        in_specs=[col(0), col(1), col(2), col(0)] + tables + [pl.BlockSpec((GRID_W, GRID_W), lambda g: (0, 0)),
                                                              pl.BlockSpec(memory_space=pl.ANY)],
        out_specs=(pl.BlockSpec((3, T, HC), lambda g: (0, 0, g)),
                   pl.BlockSpec((1, WIN_H, HG, KWIN), lambda g: (g, 0, 0, 0))),
        scratch_shapes=[pltpu.VMEM((T, HC), F32), pltpu.VMEM((T, HC), F32), tiles, tiles],
        compiler_params=_params(dimension_semantics=("parallel",)),
    )(qkv, qkv, qkv, datt, bias_rows, jnp.asarray(valid_np), jnp.asarray(same_head_np),
      jnp.asarray(np.eye(GRID_W, dtype=np.float32)[::-1].copy()), after)


def _conv_taps(up):
    return (_shift_rows(up, 2), _shift_rows(up, 1), up, _shift_rows(up, -1))


def _pair_block_diag(w_pair, dup, same_half):
    return jnp.where(same_half, _dot(w_pair.astype(BF16), dup), 0.0).astype(BF16)


def _gates(u, u16, wa, ba, wi, bi, lam):
    r = _sigmoid(_dot(u16, wa) + ba)
    ig = _sigmoid(_dot(u16, wi) + bi)
    sp = _softplus(-lam)
    log_a = (-LRU_C) * r * sp
    a = jnp.exp(log_a)
    mult2 = jnp.maximum(_one_minus_square(log_a, a), 0.0)
    return r, ig, sp, a, jnp.sqrt(mult2), mult2


SCAN_BLOCKS = 8


def _scans(jobs):
    c = jobs[0][0].shape[1]
    nblk = T // 8
    rows = lax.broadcasted_iota(jnp.int32, (8, c), 0)

    def block(a, b, reverse):
        for s in (1, 2, 4):
            if reverse:
                keep = rows < 8 - s
                a_s = jnp.where(keep, pltpu.roll(a, 8 - s, 0), 1.0)
                b_s = jnp.where(keep, pltpu.roll(b, 8 - s, 0), 0.0)
            else:
                keep = rows >= s
                a_s = jnp.where(keep, pltpu.roll(a, s, 0), 1.0)
                b_s = jnp.where(keep, pltpu.roll(b, s, 0), 0.0)
            b = a * b_s + b
            a = a * a_s
        return a, b

    def step(i, carry):
        out = []
        for (a_ref, b_ref, h_ref, reverse), h_prev in zip(jobs, carry):
            for u in range(SCAN_BLOCKS):
                blk = i * SCAN_BLOCKS + u
                if reverse:
                    blk = nblk - 1 - blk
                t0 = pl.multiple_of(blk * 8, 8)
                a, b = block(a_ref[pl.ds(t0, 8), :], b_ref[pl.ds(t0, 8), :], reverse)
                h = a * h_prev + b
                h_ref[pl.ds(t0, 8), :] = h
                h_prev = jnp.broadcast_to(h[0:1] if reverse else h[7:8], (8, c))
            out.append(h_prev)
        return tuple(out)

    lax.fori_loop(0, nblk // SCAN_BLOCKS, step, tuple(jnp.zeros((8, c), F32) for _ in jobs))


def _rec_specs():
    tok = lambda off: pl.BlockSpec((T, CG), lambda g: (0, g + off))
    per_ch = lambda rows: pl.BlockSpec((rows, CG), lambda g: (0, g))
    wspec = pl.BlockSpec((2, 1, CG, REC_BLOCK), lambda g: (0, g, 0, 0))
    const = lambda shape: pl.BlockSpec(shape, lambda g: (0, 0))
    return tok, per_ch, wspec, const


def _rec_fwd(uy, conv_w, conv_b, w_a, b_a, w_i, b_i, lam):
    tok, per_ch, wspec, const = _rec_specs()

    def body(up_ref, yb_ref, cw_ref, cb_ref, wa_ref, ba_ref, wi_ref, bi_ref, lam_ref, dup_ref, half_ref,
             hf_ref, hb_ref, yrec_ref, am_ref, bx_f, bx_b):
        dup = dup_ref[...]
        same_half = half_ref[...] > 0.5
        taps = _conv_taps(up_ref[...])
        u = cb_ref[...]
        for j in range(4):
            u = u + taps[j] * cw_ref[j:j + 1, :]
        u16 = u.astype(BF16)
        for d, bx_s in enumerate((bx_f, bx_b)):
            wa = _pair_block_diag(wa_ref[d, 0], dup, same_half)
            wi = _pair_block_diag(wi_ref[d, 0], dup, same_half)
            _, ig, _, a, mult, _ = _gates(u, u16, wa, ba_ref[d:d + 1, :], wi, bi_ref[d:d + 1, :],
                                       lam_ref[d:d + 1, :])
            am_ref[2 * d] = a
            am_ref[2 * d + 1] = mult
            bx_s[...] = mult * (ig * u)
        _scans([(am_ref.at[0], bx_f, hf_ref, False), (am_ref.at[2], bx_b, hb_ref, True)])
        gelu, _ = _gelu_and_grad(yb_ref[...])
        yrec_ref[...] = ((hf_ref[...] + hb_ref[...]) * gelu).astype(BF16)

    return pl.pallas_call(
        body, name="rec_fwd",
        out_shape=(jax.ShapeDtypeStruct((T, D_REC), F32), jax.ShapeDtypeStruct((T, D_REC), F32),
                   jax.ShapeDtypeStruct((T, D_REC), BF16), jax.ShapeDtypeStruct((4, T, D_REC), F32)),
        grid=(N_CG,),
        in_specs=[tok(0), tok(N_CG), per_ch(4), per_ch(1), wspec, per_ch(2), wspec, per_ch(2), per_ch(2),
                  const((REC_BLOCK, CG)), const((CG, CG))],
        out_specs=(tok(0), tok(0), tok(0), pl.BlockSpec((4, T, CG), lambda g: (0, 0, g))),
        scratch_shapes=[pltpu.VMEM((T, CG), F32)] * 2,
        compiler_params=_params(dimension_semantics=("parallel",)),
    )(uy, uy, conv_w, conv_b, w_a, b_a, w_i, b_i, lam,
      jnp.asarray(_dup_table(), BF16), jnp.asarray(_pair_mask()))


def _rec_bwd(uy, hf, hb, am, dyrec, conv_w, conv_b, w_a, b_a, w_i, b_i, lam, after):
    tok, per_ch, wspec, const = _rec_specs()

    def body(up_ref, yb_ref, hf_ref, hb_ref, am_ref, dy_ref, cw_ref, cb_ref, wa_ref, ba_ref, wi_ref, bi_ref,
             lam_ref, dup_ref, dupt_ref, half_ref, after_ref,
             duy_ref, dcw_ref, dcb_ref, dwa_ref, dba_ref, dwi_ref, dbi_ref, dlam_ref,
             a_s0, a_s1, dh_s, g_s0, g_s1):
        dup = dup_ref[...]
        dup_t = dupt_ref[...]
        same_half = half_ref[...] > 0.5
        taps = _conv_taps(up_ref[...])
        u = cb_ref[...]
        for j in range(4):
            u = u + taps[j] * cw_ref[j:j + 1, :]
        u16 = u.astype(BF16)
        gelu, dgelu = _gelu_and_grad(yb_ref[...])
        dy = dy_ref[...]
        duy_ref[1] = (dy * (hf_ref[...] + hb_ref[...]) * dgelu).astype(BF16)
        dh_s[...] = dy * gelu
        a_s0[...] = _shift_rows(am_ref[0], -1)
        a_s1[...] = _shift_rows(am_ref[2], 1)
        _scans([(a_s0, dh_s, g_s0, True), (a_s1, dh_s, g_s1, False)])
        du = jnp.zeros((T, CG), F32)
        for d, g_s in enumerate((g_s0, g_s1)):
            reverse = d == 1
            wa = _pair_block_diag(wa_ref[d, 0], dup, same_half)
            wi = _pair_block_diag(wi_ref[d, 0], dup, same_half)
            lam_d = lam_ref[d:d + 1, :]
            r = _sigmoid(_dot(u16, wa) + ba_ref[d:d + 1, :])
            ig = _sigmoid(_dot(u16, wi) + bi_ref[d:d + 1, :])
            sp = _softplus(-lam_d)
            a, mult = am_ref[2 * d], am_ref[2 * d + 1]
            mult2 = mult * mult
            g = g_s[...]
            h_prev = _shift_rows(hb_ref[...], -1) if reverse else _shift_rows(hf_ref[...], 1)
            da = g * h_prev
            dmult = g * (ig * u)
            dig = g * mult * u
            du = du + g * mult * ig
            dmult_dlog = jnp.where(mult2 > 0.0, -(a * a) * lax.rsqrt(mult2), 0.0)
            dlog_a = da * a + dmult * dmult_dlog
            dr = dlog_a * ((-LRU_C) * sp)
            dsp = jnp.sum(dlog_a * ((-LRU_C) * r), axis=0, keepdims=True)
            dlam_ref[d:d + 1, :] = dsp * (-_sigmoid(-lam_d))
            dga = dr * r * (1.0 - r)
            dgi = dig * ig * (1.0 - ig)
            dga16 = dga.astype(BF16)
            dgi16 = dgi.astype(BF16)
            du = du + _dot_nt(dga16, wa) + _dot_nt(dgi16, wi)
            dwa_ref[d, 0] = _dot_exact(jnp.where(same_half, _dot_tn(u16, dga16), 0.0), dup_t)
            dwi_ref[d, 0] = _dot_exact(jnp.where(same_half, _dot_tn(u16, dgi16), 0.0), dup_t)
            dba_ref[d:d + 1, :] = jnp.sum(dga, axis=0, keepdims=True)
            dbi_ref[d:d + 1, :] = jnp.sum(dgi, axis=0, keepdims=True)
        dcb_ref[...] = jnp.sum(du, axis=0, keepdims=True)
        for j in range(4):
            dcw_ref[j:j + 1, :] = jnp.sum(du * taps[j], axis=0, keepdims=True)
        dup_in = (_shift_rows(du, -2) * cw_ref[0:1, :] + _shift_rows(du, -1) * cw_ref[1:2, :]
                  + du * cw_ref[2:3, :] + _shift_rows(du, 1) * cw_ref[3:4, :])
        duy_ref[0] = dup_in.astype(BF16)

    wshape = jax.ShapeDtypeStruct((2, N_CG, CG, REC_BLOCK), F32)
    vec = lambda rows: jax.ShapeDtypeStruct((rows, D_REC), F32)
    dup_np = _dup_table()
    return pl.pallas_call(
        body, name="rec_bwd",
        out_shape=(jax.ShapeDtypeStruct((2, T, D_REC), BF16),
                   vec(4), vec(1), wshape, vec(2), wshape, vec(2), vec(2)),
        grid=(N_CG,),
        in_specs=[tok(0), tok(N_CG), tok(0), tok(0), pl.BlockSpec((4, T, CG), lambda g: (0, 0, g)), tok(0),
                  per_ch(4), per_ch(1), wspec, per_ch(2), wspec, per_ch(2), per_ch(2),
                  const((REC_BLOCK, CG)), const((CG, REC_BLOCK)), const((CG, CG)),
                  pl.BlockSpec(memory_space=pl.ANY)],
        out_specs=(pl.BlockSpec((2, T, CG), lambda g: (0, 0, g)),
                   per_ch(4), per_ch(1), wspec, per_ch(2), wspec, per_ch(2), per_ch(2)),
        scratch_shapes=[pltpu.VMEM((T, CG), F32)] * 5,
        compiler_params=_params(dimension_semantics=("parallel",)),
    )(uy, uy, hf, hb, am, dyrec, conv_w, conv_b, w_a, b_a, w_i, b_i, lam,
      jnp.asarray(dup_np, BF16), jnp.asarray(dup_np.T.copy()), jnp.asarray(_pair_mask()), after)


TM_MIX = 256


def _mix_specs():
    tok = lambda width, blk=0: pl.BlockSpec((TM_MIX, width), lambda i: (i, blk))
    full = lambda shape: pl.BlockSpec(shape, lambda i: (0, 0))
    return tok, full


def _mix_fwd(x, att, yrec, gg, w_att_o_t, w_rec_o, w_out):
    tok, full = _mix_specs()

    def body(x_ref, att_ref, yr_ref, ga_ref, gr_ref, wao_ref, wro_ref, wo_ref, x1_ref, mixed_ref):
        y_att = _dot_nt(att_ref[...], wao_ref[...])
        y_rec = _dot(yr_ref[...], wro_ref[...])
        mixed = (_sigmoid(ga_ref[...]) * y_att + _sigmoid(gr_ref[...]) * y_rec).astype(BF16)
        mixed_ref[...] = mixed
        x1_ref[...] = x_ref[...] + _dot(mixed, wo_ref[...])

    return pl.pallas_call(
        body, name="mix_fwd",
        out_shape=(jax.ShapeDtypeStruct((T, D), F32), jax.ShapeDtypeStruct((T, D), BF16)),
        grid=(T // TM_MIX,),
        in_specs=[tok(D), tok(D_ATT), tok(D_REC), tok(D, 0), tok(D, 1),
                  full((D, D_ATT)), full((D_REC, D)), full((D, D))],
        out_specs=(tok(D), tok(D)),
        compiler_params=_params(dimension_semantics=("parallel",)),
    )(x, att, yrec, gg, gg, w_att_o_t, w_rec_o, w_out)


def _mix_bwd(dx1, att, yrec, gg, w_att_o_t, w_rec_o, w_out, after):
    tok, full = _mix_specs()

    def body(dx_ref, att_ref, yr_ref, ga_ref, gr_ref, wao_ref, wro_ref, wo_ref, after_ref,
             dgg_ref, dya_ref, dyr_ref, datt_ref, dyrp_ref):
        dmixed = _dot_nt(dx_ref[...].astype(BF16), wo_ref[...])
        y_att = _dot_nt(att_ref[...], wao_ref[...])
        y_rec = _dot(yr_ref[...], wro_ref[...])
        sa = _sigmoid(ga_ref[...])
        sr = _sigmoid(gr_ref[...])
        dgg_ref[0] = (dmixed * y_att * sa * (1.0 - sa)).astype(BF16)
        dgg_ref[1] = (dmixed * y_rec * sr * (1.0 - sr)).astype(BF16)
        dya = (dmixed * sa).astype(BF16)
        dyr = (dmixed * sr).astype(BF16)
        dya_ref[...] = dya
        dyr_ref[...] = dyr
        datt_ref[...] = _dot(dya, wao_ref[...]).astype(BF16)
        dyrp_ref[...] = _dot_nt(dyr, wro_ref[...])

    return pl.pallas_call(
        body, name="mix_bwd",
        out_shape=(jax.ShapeDtypeStruct((2, T, D), BF16),
                   jax.ShapeDtypeStruct((T, D), BF16), jax.ShapeDtypeStruct((T, D), BF16),
                   jax.ShapeDtypeStruct((T, D_ATT), BF16), jax.ShapeDtypeStruct((T, D_REC), F32)),
        grid=(T // TM_MIX,),
        in_specs=[tok(D), tok(D_ATT), tok(D_REC), tok(D, 0), tok(D, 1),
                  full((D, D_ATT)), full((D_REC, D)), full((D, D)), pl.BlockSpec(memory_space=pl.ANY)],
        out_specs=(pl.BlockSpec((2, TM_MIX, D), lambda i: (0, i, 0)),
                   tok(D), tok(D), tok(D_ATT), tok(D_REC)),
        compiler_params=_params(dimension_semantics=("parallel",)),
    )(dx1, att, yrec, gg, gg, w_att_o_t, w_rec_o, w_out, after)


TM_FFN = 256
FF_CHUNK = 1024


def _ffn_loss(x1, target, g2, gf, w_ff1_t, w_ff2):
    n_chunks = D_FF // FF_CHUNK

    def body(x1_ref, tg_ref, g2_ref, gf_ref, w1_hbm, w2_hbm,
             loss_ref, dx1_ref, h2_ref, act_ref, dpre_ref, dx2_ref, dg2_ref, dgf_ref,
             w1, w2, relu_s):
        i = pl.program_id(0)

        @pl.when(i == 0)
        def _():
            pltpu.sync_copy(w1_hbm, w1)
            pltpu.sync_copy(w2_hbm, w2)
            loss_ref[...] = jnp.zeros_like(loss_ref)
            dg2_ref[...] = jnp.zeros_like(dg2_ref)
            dgf_ref[...] = jnp.zeros_like(dgf_ref)

        x1v = x1_ref[...]
        r2 = lax.rsqrt(jnp.mean(x1v * x1v, axis=-1, keepdims=True) + EPS)
        xh2 = x1v * r2
        h2 = (xh2 * g2_ref[...]).astype(BF16)
        h2_ref[...] = h2
        x2 = x1v
        for c in range(n_chunks):
            ff = slice(c * FF_CHUNK, (c + 1) * FF_CHUNK)
            rl = jnp.maximum(_dot_nt(h2, w1[ff, :]), 0.0)
            relu_s[:, ff] = rl
            act = (rl * rl).astype(BF16)
            act_ref[:, ff] = act
            x2 = x2 + _dot(act, w2[ff, :])
        r3 = lax.rsqrt(jnp.mean(x2 * x2, axis=-1, keepdims=True) + EPS)
        xh3 = x2 * r3
        err = xh3 * gf_ref[...] - tg_ref[...]
        loss_ref[...] += 0.5 * jnp.sum(jnp.mean(err * err, axis=-1, keepdims=True))
        dy = err * (1.0 / D)
        dgf_ref[...] += jnp.sum(dy * xh3, axis=0, keepdims=True)
        dx2 = _rms_bwd(dy, xh3, r3, gf_ref[...])
        dx2_16 = dx2.astype(BF16)
        dx2_ref[...] = dx2_16
        dh2 = jnp.zeros((TM_FFN, D), F32)
        for c in range(n_chunks):
            ff = slice(c * FF_CHUNK, (c + 1) * FF_CHUNK)
            dpre = (_dot_nt(dx2_16, w2[ff, :]) * (2.0 * relu_s[:, ff])).astype(BF16)
            dpre_ref[:, ff] = dpre
            dh2 = dh2 + _dot(dpre, w1[ff, :])
        dg2_ref[...] += jnp.sum(dh2 * xh2, axis=0, keepdims=True)
        dx1_ref[...] = dx2 + _rms_bwd(dh2, xh2, r2, g2_ref[...])

    tok = lambda width: pl.BlockSpec((TM_FFN, width), lambda i: (i, 0))
    vec = pl.BlockSpec((1, D), lambda i: (0, 0))
    hbm = pl.BlockSpec(memory_space=pl.ANY)
    return pl.pallas_call(
        body, name="ffn_loss",
        out_shape=(jax.ShapeDtypeStruct((8, 128), F32), jax.ShapeDtypeStruct((T, D), F32),
                   jax.ShapeDtypeStruct((T, D), BF16), jax.ShapeDtypeStruct((T, D_FF), BF16),
                   jax.ShapeDtypeStruct((T, D_FF), BF16), jax.ShapeDtypeStruct((T, D), BF16),
                   jax.ShapeDtypeStruct((1, D), F32), jax.ShapeDtypeStruct((1, D), F32)),
        grid=(T // TM_FFN,),
        in_specs=[tok(D), tok(D), vec, vec, hbm, hbm],
        out_specs=(pl.BlockSpec((8, 128), lambda i: (0, 0)), tok(D), tok(D), tok(D_FF), tok(D_FF), tok(D),
                   vec, vec),
        scratch_shapes=[pltpu.VMEM((D_FF, D), BF16), pltpu.VMEM((D_FF, D), BF16),
                        pltpu.VMEM((TM_FFN, D_FF), F32)],
        compiler_params=_params(dimension_semantics=("arbitrary",)),
    )(x1, target, g2, gf, w_ff1_t, w_ff2)


def _local_step(x, target, p, late_weights, late_weights_ready, reduce_first, reduce_early, reduce_late):
    bias = _rpb_rows(p["rpb"])
    pairs = lambda w: w.reshape(2, N_CG, CG, REC_BLOCK)
    w_a, w_i = pairs(p["w_rg_a"]), pairs(p["w_rg_i"])
    rec_params = (p["conv_w"], p["conv_b"], w_a, p["b_rg_a"], w_i, p["b_rg_i"], p["lru_lambda"])

    qkv, uy, gg, h = _in_proj(x, p["ln1_g"], p["w_in_t"], p["b_in"], p["later_weights_started"])
    hf, hb, yrec, am = _rec_fwd(uy, *rec_params)
    att = _att_fwd(qkv, bias, late_weights(yrec, 0))
    p = {**p, **late_weights_ready(late_weights(att, 1), 0)}
    x1, mixed = _mix_fwd(x, att, yrec, gg, p["w_att_o_t"], p["w_rec_o"], p["w_out"])
    p = {**p, **late_weights_ready(x1, 1)}
    loss8, dx1, h2, act, dpre, dx2, g_ln2, g_lnf = _ffn_loss(
        x1, target, p["ln2_g"], p["lnf_g"], p["w_ff1_t"], p["w_ff2"])

    grads = {"ln2_g": g_ln2, "lnf_g": g_lnf,
             "w_ff1_t": _matmul(dpre, h2, "tn", BF16, "g_w_ff1", tm=1024),
             "w_ff2": _matmul(act, dx2, "tn", BF16, "g_w_ff2", tm=1024)}
    dgg, dya, dyr, datt, dyrp = _mix_bwd(dx1, att, yrec, gg, p["w_att_o_t"], p["w_rec_o"], p["w_out"],
                                         reduce_first(grads, None))
    duy, g_cw, g_cb, g_wa, g_ba, g_wi, g_bi, g_lam = _rec_bwd(uy, hf, hb, am, dyrp, *rec_params,
                                                              reduce_first(None, dgg))
    blocks = lambda g: g.reshape(2, N_REC_BLOCKS, REC_BLOCK, REC_BLOCK)
    grads.update({
        "w_att_o_t": _matmul(dya, att, "tn", BF16, "g_w_att_o"),
        "conv_w": g_cw, "conv_b": g_cb, "w_rg_a": blocks(g_wa), "b_rg_a": g_ba,
        "w_rg_i": blocks(g_wi), "b_rg_i": g_bi, "lru_lambda": g_lam,
        "w_rec_o": _matmul(yrec, dyr, "tn", BF16, "g_w_rec_o"),
        "w_out": _matmul(mixed, dx1, "tn", BF16, "g_w_out"),
    })
    dqkv, gbias = _att_bwd(qkv, bias, datt, reduce_early(grads))
    dz = (dqkv, duy, dgg)
    g_w_in_t, g_b_in = _grad_w_in(dz, h)
    grads.update(w_in_t=g_w_in_t, b_in=g_b_in)
    grad_x, g_ln1 = _dh_norm1_bwd(dz, p["w_in_t"], x, p["ln1_g"], dx1, reduce_late(grads))
    grads.update(ln1_g=g_ln1, rpb=_rpb_fold(gbias))
    return loss8[0:1, 0:1], grad_x, grads


MESH_ID = pl.DeviceIdType.MESH
ANY = pl.BlockSpec(memory_space=pl.ANY)

CHAN_BLOCK_ROWS = 32
GATE_ROWS = 2 * 2 * N_REC_BLOCKS * REC_BLOCK * REC_BLOCK // (N_DEV * D)
SECTIONS = (("w_in_t", 704, D), ("w_rec_o", 128, D), ("w_out", 128, D), ("w_ff1_t", 512, D),
            ("w_ff2", 512, D), ("chan", CHAN_BLOCK_ROWS, D), ("w_att_o_t", 128, D_ATT),
            ("gates", GATE_ROWS, D))
N_SEC = len(SECTIONS)
N_CHAN_ROWS = 10
CHAN = (("conv_w", 4), ("b_rg_a", 2), ("b_rg_i", 2), ("lru_lambda", 2))


def _position():
    return lax.axis_index("x"), lax.axis_index("y"), lax.axis_index("c")


def _other_chips(x, y):
    return [(1 - x, y), (x, 1 - y), (1 - x, 1 - y)]


PASS_ON_IDS, PAIR_EARLY_ID, PAIR_LATE_ID, PAIR_FIRST_ID, SMALL_PASS_ON_ID = (1, 4), 2, 3, 5, 6


def _pair_handshake(x, y, c):
    barrier = pltpu.get_barrier_semaphore()
    pl.semaphore_signal(barrier, inc=1, device_id=(x, y, 1 - c), device_id_type=MESH_ID)
    pl.semaphore_wait(barrier, 1)


def _block_of(ref, dev, rows):
    return ref.at[pl.ds(pl.multiple_of(dev * rows, 16), rows)]


def _all_gather(shards, name):
    ns = len(shards)

    def body(*refs):
        x_refs, out_refs = refs[:ns], refs[ns:2 * ns]
        send_sems, recv_sems, local_sems = refs[2 * ns:]
        x, y, c = _position()
        me, sibling = (x, y, c), (x, y, 1 - c)
        x_nbr, y_nbr, diagonal = _other_chips(x, y)
        north = c == 1
        relay_from = (jnp.where(north, x_nbr[0], y_nbr[0]), jnp.where(north, x_nbr[1], y_nbr[1]))
        relay_to = (jnp.where(north, y_nbr[0], x_nbr[0]), jnp.where(north, y_nbr[1], x_nbr[1]))

        def rows(s, px, py, pc):
            return _block_of(out_refs[s], 4 * px + 2 * py + pc, shards[s].shape[0])

        def copy(k, s, block, to, from_shard=False):
            return pltpu.make_async_remote_copy(
                src_ref=x_refs[s] if from_shard else rows(s, *block), dst_ref=rows(s, *block),
                send_sem=send_sems.at[k * ns + s], recv_sem=recv_sems.at[k * ns + s],
                device_id=to, device_id_type=MESH_ID)

        sections = range(ns)
        mine = [pltpu.make_async_copy(x_refs[s], rows(s, *me), local_sems.at[s]) for s in sections]
        sent = [copy(k, s, me, to, True) for k, to in enumerate((sibling, (*x_nbr, c), (*y_nbr, c)))
                for s in sections]
        for cp in mine + sent:
            cp.start()
        for s in sections:
            copy(1, s, (*x_nbr, c), me).wait_recv()
            copy(2, s, (*y_nbr, c), me).wait_recv()
            sent += [copy(3, s, (*relay_from, c), (*relay_to, c)),
                     copy(4, s, (*x_nbr, c), sibling), copy(5, s, (*y_nbr, c), sibling)]
            for cp in sent[-3:]:
                cp.start()
        for s in sections:
            copy(3, s, (*diagonal, c), me).wait_recv()
            sent.append(copy(6, s, (*diagonal, c), sibling))
            sent[-1].start()
        for s in sections:
            copy(0, s, sibling, me).wait_recv()
            for k, chip in ((4, x_nbr), (5, y_nbr), (6, diagonal)):
                copy(k, s, (*chip, 1 - c), me).wait_recv()
        for cp in sent:
            cp.wait_send()
        for cp in mine:
            cp.wait()

    return pl.pallas_call(
        body, name=name,
        out_shape=tuple(jax.ShapeDtypeStruct((N_DEV * s.shape[0], s.shape[1]), s.dtype) for s in shards),
        in_specs=[ANY] * ns,
        out_specs=(ANY,) * ns,
        scratch_shapes=[pltpu.SemaphoreType.DMA((7 * ns,)), pltpu.SemaphoreType.DMA((7 * ns,)),
                        pltpu.SemaphoreType.DMA((ns,))],
    )(*shards)


HBM = pl.BlockSpec(memory_space=pltpu.HBM)
SEM = pl.BlockSpec(memory_space=pltpu.SEMAPHORE)
EFFECT = pltpu.SideEffectType.DATAFLOW_SIDE_EFFECTING


def _in_hbm(a):
    return pltpu.with_memory_space_constraint(a, pltpu.HBM)


def _first_hop_copies(rows, which, x_refs, zones, send_sems, recv_sems):
    ns = len(rows)
    x, y, c = _position()
    targets = [(x, y, 1 - c)] + [(cx, cy, c) for cx, cy in _other_chips(x, y)]
    return [pltpu.make_async_remote_copy(
        src_ref=x_refs[i], dst_ref=_block_of(zones[i], 4 * x + 2 * y + c, rows[s]),
        send_sem=send_sems.at[k * ns + s], recv_sem=recv_sems.at[k * ns + s],
        device_id=to, device_id_type=MESH_ID)
        for k, to in enumerate(targets) for i, s in enumerate(which)]


def _after_all(arrays, name):
    def body(*refs):
        refs[-1][...] = jnp.zeros_like(refs[-1])

    return pl.pallas_call(
        body, name=name,
        out_shape=jax.ShapeDtypeStruct((8, LANES), F32),
        in_specs=[pl.BlockSpec(memory_space=pl.ANY)] * len(arrays),
        out_specs=pl.BlockSpec(memory_space=pltpu.VMEM),
    )(*arrays)


def _own_blocks_placed(shards, after, name):
    ns = len(shards)
    x, y, c = _position()
    me = jnp.reshape(4 * x + 2 * y + c, (1,)).astype(jnp.int32)
    tokens = [] if after is None else [after]

    def body(me_ref, *refs):
        for s in range(ns):
            refs[ns + len(tokens) + s][...] = refs[s][...]

    return pl.pallas_call(
        body, name=name,
        out_shape=tuple(jax.ShapeDtypeStruct((N_DEV * s.shape[0], s.shape[1]), s.dtype) for s in shards),
        grid_spec=pltpu.PrefetchScalarGridSpec(
            num_scalar_prefetch=1, grid=(1,),
            in_specs=[pl.BlockSpec(s.shape, lambda i, me: (0, 0)) for s in shards] + [ANY] * len(tokens),
            out_specs=tuple(pl.BlockSpec(s.shape, lambda i, me: (me[0], 0)) for s in shards)),
        compiler_params=_params(dimension_semantics=("arbitrary",)),
    )(me, *shards, *tokens)


def _gather_start(shards, after, name):
    ns = len(shards)
    zones = _own_blocks_placed(shards, after, name + "_own_blocks")

    def body(*refs):
        for cp in _first_hop_copies([s.shape[0] for s in shards], range(ns), refs[:ns], refs[ns:2 * ns],
                                    refs[2 * ns], refs[2 * ns + 1]):
            cp.start()
        refs[-1][...] = jnp.zeros_like(refs[-1])

    out = pl.pallas_call(
        body, name=name,
        out_shape=(pltpu.SemaphoreType.DMA((4 * ns,)), pltpu.SemaphoreType.DMA((4 * ns,)),
                   *[pltpu.HBM(a.shape, a.dtype) for a in (*shards, *zones)],
                   jax.ShapeDtypeStruct((8, LANES), F32)),
        in_specs=[HBM] * (2 * ns),
        out_specs=(SEM, SEM, *[HBM] * (2 * ns), pl.BlockSpec(memory_space=pltpu.VMEM)),
        input_output_aliases={i: 2 + i for i in range(2 * ns)},
        compiler_params=pltpu.CompilerParams(has_side_effects=EFFECT),
    )(*[_in_hbm(a) for a in shards], *[_in_hbm(a) for a in zones])
    return out[0], out[1], out[2:2 + ns], out[2 + ns:2 + 2 * ns], out[-1]


def _gather_wait(send_sems, recv_sems, rows, which, shards, zones, after, name):
    ns = len(shards)

    def body(*refs):
        for cp in _first_hop_copies(rows, which, refs[:ns], refs[ns:2 * ns], refs[2 * ns], refs[2 * ns + 1]):
            cp.wait_send()
            cp.wait_recv()

    out = pl.pallas_call(
        body, name=name,
        out_shape=tuple(pltpu.HBM(a.shape, a.dtype) for a in (*shards, *zones)),
        in_specs=[HBM] * (2 * ns) + [SEM, SEM, ANY],
        out_specs=(HBM,) * (2 * ns),
        input_output_aliases={i: i for i in range(2 * ns)},
        compiler_params=pltpu.CompilerParams(has_side_effects=EFFECT),
    )(*shards, *zones, send_sems, recv_sems, after)
    return out[:ns], out[ns:]


def _pass_on_copies(rows, in_refs, out_refs, send_sems, recv_sems):
    ns = len(rows)
    x, y, c = _position()
    return [pltpu.make_async_remote_copy(
        src_ref=_block_of(in_refs[s], 4 * cx + 2 * cy + c, rows[s]),
        dst_ref=_block_of(out_refs[s], 4 * cx + 2 * cy + c, rows[s]),
        send_sem=send_sems.at[j * ns + s], recv_sem=recv_sems.at[j * ns + s],
        device_id=(x, y, 1 - c), device_id_type=MESH_ID)
        for j, (cx, cy) in enumerate(_other_chips(x, y)) for s in range(ns)]


def _pass_on_start(rows, zones, barrier_id, name):
    ns = len(zones)

    def body(*refs):
        _pair_handshake(*_position())
        for cp in _pass_on_copies(rows, refs[:ns], refs[:ns], refs[ns], refs[ns + 1]):
            cp.start()
        refs[-1][...] = jnp.zeros_like(refs[-1])

    out = pl.pallas_call(
        body, name=name,
        out_shape=(pltpu.SemaphoreType.DMA((3 * ns,)), pltpu.SemaphoreType.DMA((3 * ns,)),
                   *[pltpu.HBM(z.shape, z.dtype) for z in zones], jax.ShapeDtypeStruct((8, LANES), F32)),
        in_specs=[HBM] * ns,
        out_specs=(SEM, SEM, *[HBM] * ns, pl.BlockSpec(memory_space=pltpu.VMEM)),
        input_output_aliases={i: 2 + i for i in range(ns)},
        compiler_params=pltpu.CompilerParams(has_side_effects=EFFECT, collective_id=barrier_id),
    )(*[_in_hbm(z) for z in zones])
    return out[0], out[1], out[2:2 + ns], out[-1]


def _pass_on_wait(rows, send_sems, recv_sems, zones, after, name):
    ns = len(zones)

    def body(*refs):
        for cp in _pass_on_copies(rows, refs[:ns], refs[:ns], refs[ns], refs[ns + 1]):
            cp.wait_send()
            cp.wait_recv()

    return pl.pallas_call(
        body, name=name,
        out_shape=tuple(pltpu.HBM(z.shape, z.dtype) for z in zones),
        in_specs=[HBM] * ns + [SEM, SEM, ANY],
        out_specs=(HBM,) * ns,
        input_output_aliases={i: i for i in range(ns)},
        compiler_params=pltpu.CompilerParams(has_side_effects=EFFECT),
    )(*zones, send_sems, recv_sems, after)


def _gather_pass_on(rows, zones, barrier_id, name):
    ns = len(zones)

    def body(*refs):
        _pair_handshake(*_position())
        copies = _pass_on_copies(rows, refs[:ns], refs[ns:2 * ns], *refs[2 * ns:])
        for cp in copies:
            cp.start()
        for cp in copies:
            cp.wait_recv()
        for cp in copies:
            cp.wait_send()

    return pl.pallas_call(
        body, name=name,
        out_shape=tuple(jax.ShapeDtypeStruct(z.shape, z.dtype) for z in zones),
        in_specs=[ANY] * ns, out_specs=(ANY,) * ns,
        input_output_aliases={i: i for i in range(ns)},
        scratch_shapes=[pltpu.SemaphoreType.DMA((3 * ns,)), pltpu.SemaphoreType.DMA((3 * ns,))],
        compiler_params=pltpu.CompilerParams(collective_id=barrier_id),
    )(*zones)


def _pair_copies(sections, g_refs, land, send_sems, recv_sems):
    ns = len(sections)
    x, y, c = _position()
    return [pltpu.make_async_remote_copy(
        src_ref=_block_of(g_refs[s], 2 * k + 1 - c, rows), dst_ref=land[s].at[k],
        send_sem=send_sems.at[k * ns + s], recv_sem=recv_sems.at[k * ns + s],
        device_id=(x, y, 1 - c), device_id_type=MESH_ID)
        for k in range(N_CHIPS) for s, (_, rows, _) in enumerate(sections)]


def _pair_exchange_start(sections, grads, barrier_id, name):
    ns = len(sections)

    def body(*refs):
        _pair_handshake(*_position())
        for cp in _pair_copies(sections, refs[:ns], refs[ns:2 * ns], refs[2 * ns], refs[2 * ns + 1]):
            cp.start()
        refs[-1][...] = jnp.zeros_like(refs[-1])

    zones = [lax.empty((N_CHIPS, rows, cols), BF16) for _, rows, cols in sections]
    n = N_CHIPS * ns
    out = pl.pallas_call(
        body, name=name,
        out_shape=(pltpu.SemaphoreType.DMA((n,)), pltpu.SemaphoreType.DMA((n,)),
                   *[pltpu.HBM(a.shape, a.dtype) for a in (*grads, *zones)],
                   jax.ShapeDtypeStruct((8, LANES), F32)),
        in_specs=[HBM] * (2 * ns),
        out_specs=(SEM, SEM, *[HBM] * (2 * ns), pl.BlockSpec(memory_space=pltpu.VMEM)),
        input_output_aliases={i: 2 + i for i in range(2 * ns)},
        compiler_params=pltpu.CompilerParams(has_side_effects=EFFECT, collective_id=barrier_id),
    )(*[_in_hbm(a) for a in grads], *[_in_hbm(a) for a in zones])
    return out[0], out[1], out[2:2 + ns], out[2 + ns:2 + 2 * ns], out[-1]


def _pair_exchange_wait(sections, send_sems, recv_sems, grads, zones, after, name):
    ns = len(sections)

    def body(*refs):
        for cp in _pair_copies(sections, refs[:ns], refs[ns:2 * ns], refs[2 * ns], refs[2 * ns + 1]):
            cp.wait_send()
            cp.wait_recv()

    out = pl.pallas_call(
        body, name=name,
        out_shape=tuple(pltpu.HBM(a.shape, a.dtype) for a in (*grads, *zones)),
        in_specs=[HBM] * (2 * ns) + [SEM, SEM, ANY],
        out_specs=(HBM,) * (2 * ns),
        input_output_aliases={i: i for i in range(2 * ns)},
        compiler_params=pltpu.CompilerParams(has_side_effects=EFFECT),
    )(*grads, *zones, send_sems, recv_sems, after)
    return out[:ns], out[ns:]


def _pair_add(sections, grads, got, core, name):
    ns = len(sections)

    def body(core_ref, *refs):
        g_refs, got_refs, p_refs = refs[:ns], refs[ns:2 * ns], refs[2 * ns:]
        for s in range(ns):
            p_refs[s][0] = (g_refs[s][...].astype(F32) + got_refs[s][0].astype(F32)).astype(BF16)

    slot = [pl.BlockSpec((1, rows, cols), lambda k, c: (k, 0, 0)) for _, rows, cols in sections]
    return pl.pallas_call(
        body, name=name,
        out_shape=tuple(jax.ShapeDtypeStruct((N_CHIPS, rows, cols), BF16) for _, rows, cols in sections),
        grid_spec=pltpu.PrefetchScalarGridSpec(
            num_scalar_prefetch=1, grid=(N_CHIPS,),
            in_specs=[pl.BlockSpec((rows, cols), lambda k, c: (2 * k + c[0], 0)) for _, rows, cols in sections]
            + slot,
            out_specs=tuple(slot)),
        compiler_params=_params(dimension_semantics=("parallel",)),
    )(core, *grads, *got)


def _chip_copies(sections, p_refs, land, send_sems, recv_sems):
    ns = len(sections)
    x, y, c = _position()
    return [pltpu.make_async_remote_copy(
        src_ref=p_refs[s].at[2 * cx + cy], dst_ref=land[s].at[j],
        send_sem=send_sems.at[j * ns + s], recv_sem=recv_sems.at[j * ns + s],
        device_id=(cx, cy, c), device_id_type=MESH_ID)
        for j, (cx, cy) in enumerate(_other_chips(x, y)) for s in range(ns)]


def _chip_exchange(sections, parts, name):
    ns = len(sections)

    def body(*refs):
        copies = _chip_copies(sections, refs[:ns], refs[ns:2 * ns], *refs[2 * ns:])
        for cp in copies:
            cp.start()
        for cp in copies:
            cp.wait_recv()
        for cp in copies:
            cp.wait_send()

    n = 3 * ns
    return pl.pallas_call(
        body, name=name,
        out_shape=tuple(jax.ShapeDtypeStruct((3, rows, cols), BF16) for _, rows, cols in sections),
        in_specs=[ANY] * ns, out_specs=(ANY,) * ns,
        scratch_shapes=[pltpu.SemaphoreType.DMA((n,)), pltpu.SemaphoreType.DMA((n,))],
    )(*parts)


def _chip_exchange_start(sections, parts, name):
    ns = len(sections)

    def body(*refs):
        p_refs, land = refs[:ns], refs[ns:2 * ns]
        send_sems, recv_sems = refs[2 * ns], refs[2 * ns + 1]
        token = refs[-1]
        for cp in _chip_copies(sections, p_refs, land, send_sems, recv_sems):
            cp.start()
        token[...] = jnp.zeros_like(token)

    zones = [lax.empty((3, rows, cols), BF16) for _, rows, cols in sections]
    out = pl.pallas_call(
        body, name=name,
        out_shape=(pltpu.SemaphoreType.DMA((3 * ns,)), pltpu.SemaphoreType.DMA((3 * ns,)),
                   *[pltpu.HBM(a.shape, a.dtype) for a in parts], *[pltpu.HBM(a.shape, a.dtype) for a in zones],
                   jax.ShapeDtypeStruct((8, LANES), F32)),
        in_specs=[HBM] * (2 * ns),
        out_specs=(SEM, SEM, *[HBM] * (2 * ns), pl.BlockSpec(memory_space=pltpu.VMEM)),
        input_output_aliases={i: 2 + i for i in range(2 * ns)},
        compiler_params=pltpu.CompilerParams(has_side_effects=EFFECT),
    )(*[_in_hbm(a) for a in parts], *[_in_hbm(a) for a in zones])
    return out[0], out[1], out[2:2 + ns], out[2 + ns:2 + 2 * ns], out[-1]


def _chip_exchange_wait(sections, send_sems, recv_sems, parts, zones, after, name):
    ns = len(sections)

    def body(*refs):
        p_refs, land = refs[:ns], refs[ns:2 * ns]
        for cp in _chip_copies(sections, p_refs, land, refs[2 * ns], refs[2 * ns + 1]):
            cp.wait_send()
            cp.wait_recv()

    out = pl.pallas_call(
        body, name=name,
        out_shape=tuple(pltpu.HBM(a.shape, a.dtype) for a in (*parts, *zones)),
        in_specs=[HBM] * (2 * ns) + [SEM, SEM, ANY],
        out_specs=(HBM,) * (2 * ns),
        input_output_aliases={i: i for i in range(2 * ns)},
        compiler_params=pltpu.CompilerParams(has_side_effects=EFFECT),
    )(*parts, *zones, send_sems, recv_sems, after)
    return out[:ns], out[ns:]


def _grad_finish(sections, parts, far, chip, name):
    ns = len(sections)

    def body(chip_ref, *refs):
        p_refs, b_refs, g_refs = refs[:ns], refs[ns:2 * ns], refs[2 * ns:]
        for s in range(ns):
            g = p_refs[s][0].astype(F32)
            for j in range(3):
                g = g + b_refs[s][j].astype(F32)
            g_refs[s][...] = g

    half = [(rows // 2, cols) for _, rows, cols in sections]
    return pl.pallas_call(
        body, name=name,
        out_shape=tuple(jax.ShapeDtypeStruct((rows, cols), F32) for _, rows, cols in sections),
        grid_spec=pltpu.PrefetchScalarGridSpec(
            num_scalar_prefetch=1, grid=(2,),
            in_specs=[pl.BlockSpec((1, r, c), lambda i, chip: (chip[0], i, 0)) for r, c in half]
            + [pl.BlockSpec((3, r, c), lambda i, chip: (0, i, 0)) for r, c in half],
            out_specs=tuple(pl.BlockSpec((r, c), lambda i, chip: (i, 0)) for r, c in half)),
        compiler_params=_params(dimension_semantics=("parallel",)),
    )(chip, *parts, *far)


def _sum_devices(parts, rows, name):
    cols = parts.shape[1]
    tr = rows // 2

    def body(*refs):
        s = refs[0][...].astype(F32)
        for d in range(1, N_DEV):
            s = s + refs[d][...].astype(F32)
        refs[N_DEV][...] = s

    return pl.pallas_call(
        body, name=name,
        out_shape=jax.ShapeDtypeStruct((rows, cols), F32),
        grid=(2,),
        in_specs=[pl.BlockSpec((tr, cols), lambda i, d=d: (2 * d + i, 0)) for d in range(N_DEV)],
        out_specs=pl.BlockSpec((tr, cols), lambda i: (i, 0)),
        compiler_params=_params(dimension_semantics=("parallel",)),
    )(*([parts] * N_DEV))


def _adamw_step(w_ref, g_ref, m_ref, v_ref, d_ref, nm_ref, nv_ref):
    c1 = 1.0 / (1.0 - ADAM_B1 ** ADAM_STEP)
    c2 = 1.0 / (1.0 - ADAM_B2 ** ADAM_STEP)
    gv = g_ref[...]
    nm = ADAM_B1 * m_ref[...] + (1.0 - ADAM_B1) * gv
    nv = ADAM_B2 * v_ref[...] + (1.0 - ADAM_B2) * (gv * gv)
    nm_ref[...] = nm
    nv_ref[...] = nv
    d_ref[...] = (-ADAM_LR) * ((nm * c1) / (jnp.sqrt(nv * c2) + ADAM_EPS) + ADAM_WD * w_ref[...])


def _adamw_small(params, name):
    n = len(params)

    def body(*refs):
        for k in range(n):
            _adamw_step(*refs[4 * k:4 * k + 4], *refs[4 * n + 3 * k:4 * n + 3 * k + 3])

    out = pl.pallas_call(
        body, name=name,
        out_shape=tuple(jax.ShapeDtypeStruct(p[0].shape, F32) for p in params for _ in range(3)),
    )(*[a for p in params for a in p])
    return [out[3 * k:3 * k + 3] for k in range(n)]


def _adamw(w, g, m, v, name, after=None):
    rows, cols = w.shape
    tr = rows
    while tr * cols * 4 > (1 << 20) and tr % 16 == 0:
        tr //= 2
    tokens = [] if after is None else [after]

    def body(*refs):
        _adamw_step(*refs[:4], *refs[4 + len(tokens):])

    spec = pl.BlockSpec((tr, cols), lambda i: (i, 0))
    shape = jax.ShapeDtypeStruct((rows, cols), F32)
    return pl.pallas_call(
        body, name=name,
        out_shape=(shape, shape, shape),
        grid=(rows // tr,),
        in_specs=[spec] * 4 + [ANY] * len(tokens), out_specs=(spec,) * 3,
        compiler_params=_params(dimension_semantics=("parallel",)),
    )(w, g, m, v, *tokens)


NAMES = ("ln1_g", "w_in", "b_in", "rpb", "w_att_o", "conv_w", "conv_b", "w_rg_a", "b_rg_a", "w_rg_i",
         "b_rg_i", "lru_lambda", "w_rec_o", "w_out", "ln2_g", "w_ff1", "w_ff2", "lnf_g")
TRANSPOSED = {"w_in": "w_in_t", "w_att_o": "w_att_o_t", "w_ff1": "w_ff1_t"}
ROW_SHARDED = ("w_rec_o", "w_out", "w_ff2")
REPLICATED = (("ln1_g", (1, D)), ("b_in", (1, D_IN)), ("rpb", (N_HEADS * N_RPB_R, N_RPB_C)),
              ("conv_b", (1, D_REC)), ("w_rg_a", (2 * N_REC_BLOCKS * REC_BLOCK, REC_BLOCK)),
              ("w_rg_i", (2 * N_REC_BLOCKS * REC_BLOCK, REC_BLOCK)), ("ln2_g", (1, D)), ("lnf_g", (1, D)))
GATE_BLOCKS = ("w_rg_a", "w_rg_i")
SMALL_ROWS = 112


def _chan_bits(vectors):
    chan = jnp.concatenate(vectors, axis=0)
    bits = lax.bitcast_convert_type(chan, BF16).reshape(-1)
    return jnp.pad(bits, (0, CHAN_BLOCK_ROWS * D - bits.shape[0])).reshape(CHAN_BLOCK_ROWS, D)


def _chan_from_bits(gathered):
    bits = gathered.reshape(N_DEV, CHAN_BLOCK_ROWS * D)[:, :2 * N_CHAN_ROWS * LANES]
    chan = lax.bitcast_convert_type(bits.reshape(N_DEV, N_CHAN_ROWS, LANES, 2), F32)
    return chan.transpose(1, 0, 2).reshape(N_CHAN_ROWS, D)


def kernel(x, ln1_g, w_in, b_in, rpb, w_att_o, conv_w, conv_b, w_rg_a, b_rg_a, w_rg_i, b_rg_i, lru_lambda, w_rec_o, w_out, ln2_g, w_ff1, w_ff2, lnf_g, loss_target, m_ln1_g, m_w_in, m_b_in, m_rpb, m_w_att_o, m_conv_w, m_conv_b, m_w_rg_a, m_b_rg_a, m_w_rg_i, m_b_rg_i, m_lru_lambda, m_w_rec_o, m_w_out, m_ln2_g, m_w_ff1, m_w_ff2, m_lnf_g, v_ln1_g, v_w_in, v_b_in, v_rpb, v_w_att_o, v_conv_w, v_conv_b, v_w_rg_a, v_b_rg_a, v_w_rg_i, v_b_rg_i, v_lru_lambda, v_w_rec_o, v_w_out, v_ln2_g, v_w_ff1, v_w_ff2, v_lnf_g):
    w = dict(zip(NAMES, (ln1_g, w_in, b_in, rpb, w_att_o, conv_w, conv_b, w_rg_a, b_rg_a, w_rg_i,
                         b_rg_i, lru_lambda, w_rec_o, w_out, ln2_g, w_ff1, w_ff2, lnf_g)))
    m = dict(zip(NAMES, (m_ln1_g, m_w_in, m_b_in, m_rpb, m_w_att_o, m_conv_w, m_conv_b, m_w_rg_a,
                         m_b_rg_a, m_w_rg_i, m_b_rg_i, m_lru_lambda, m_w_rec_o, m_w_out, m_ln2_g,
                         m_w_ff1, m_w_ff2, m_lnf_g)))
    v = dict(zip(NAMES, (v_ln1_g, v_w_in, v_b_in, v_rpb, v_w_att_o, v_conv_w, v_conv_b, v_w_rg_a,
                         v_b_rg_a, v_w_rg_i, v_b_rg_i, v_lru_lambda, v_w_rec_o, v_w_out, v_ln2_g,
                         v_w_ff1, v_w_ff2, v_lnf_g)))
    xi, yi, ci = _position()

    shard = {t: w[n][0].T.astype(BF16) for n, t in TRANSPOSED.items()}
    shard.update({n: w[n][0].astype(BF16) for n in ROW_SHARDED})
    shard["chan"] = _chan_bits([w[n][0] for n, _ in CHAN])
    first, later = ("w_in_t", "chan"), ("w_rec_o", "w_out", "w_att_o_t", "w_ff1_t", "w_ff2")
    p = dict(zip(first, _all_gather([shard[n] for n in first], "weight_all_gather")))
    send_sems, recv_sems, sent, zones, token = _gather_start([shard[n] for n in later], p["w_in_t"],
                                                             "weight_gather_start")

    stages = (("w_rec_o", "w_out", "w_att_o_t"), ("w_ff1_t", "w_ff2"))
    passing = {}

    def late_weights(after, stage):
        which = [later.index(n) for n in stages[stage]]
        _, arrived = _gather_wait(
            send_sems, recv_sems, [shard[n].shape[0] for n in later], which, [sent[i] for i in which],
            [zones[i] for i in which], after, "weight_gather_wait_%d" % stage)
        passing[stage] = _pass_on_start(
            [shard[n].shape[0] for n in stages[stage]], arrived,
            PASS_ON_IDS[stage], "weight_pass_on_start_%d" % stage)
        return passing[stage][-1]

    def late_weights_ready(after, stage):
        pass_send_sems, pass_recv_sems, pass_zones, _ = passing[stage]
        return dict(zip(stages[stage], _pass_on_wait(
            [shard[n].shape[0] for n in stages[stage]], pass_send_sems, pass_recv_sems, pass_zones, after,
            "weight_pass_on_wait_%d" % stage)))

    chan = _chan_from_bits(p.pop("chan"))
    r0 = 0
    for n, rows in CHAN:
        p[n] = chan[r0:r0 + rows]
        r0 += rows
    p.update(ln1_g=w["ln1_g"], b_in=w["b_in"], later_weights_started=token, rpb=w["rpb"][0], conv_b=w["conv_b"],
             w_rg_a=w["w_rg_a"][0], w_rg_i=w["w_rg_i"][0], ln2_g=w["ln2_g"],
             lnf_g=w["lnf_g"].reshape(1, D))

    core = jnp.reshape(ci, (1,)).astype(jnp.int32)
    chip = jnp.reshape(2 * xi + yi, (1,)).astype(jnp.int32)
    first_sections = tuple(s for s in SECTIONS if s[0] in ("w_ff1_t", "w_ff2"))
    late_sections = SECTIONS[:1]
    early_sections = tuple(s for s in SECTIONS[1:] if s not in first_sections)
    in_flight = {}

    def pair_sum_and_send(group, sections, after):
        send_sems, recv_sems, sect, zones, _ = in_flight["pair_" + group]
        sect, got = _pair_exchange_wait(sections, send_sems, recv_sems, sect, zones, after,
                                        "grad_pair_exchange_wait_" + group)
        parts = _pair_add(sections, sect, got, core, "grad_pair_add_" + group)
        in_flight[group] = _chip_exchange_start(sections, parts, "grad_chip_exchange_start_" + group)
        return in_flight[group][-1]

    def pair_exchange_at_once(group, sections, grads, barrier_id):
        in_flight["pair_" + group] = _pair_exchange_start(
            sections, [grads[n] for n, _, _ in sections], barrier_id, "grad_pair_exchange_start_" + group)
        return pair_sum_and_send(group, sections, in_flight["pair_" + group][-1])

    def reduce_first(grads, after):
        if grads is None:
            return pair_sum_and_send("first", first_sections, after)
        in_flight["pair_first"] = _pair_exchange_start(
            first_sections, [grads[n] for n, _, _ in first_sections], PAIR_FIRST_ID,
            "grad_pair_exchange_start_first")
        return in_flight["pair_first"][-1]

    def reduce_early(grads):
        chan_g = jnp.concatenate([grads[n] for n, _ in CHAN], axis=0)
        chan_g = chan_g.reshape(N_CHAN_ROWS, N_DEV, LANES).transpose(1, 0, 2).astype(BF16)
        chan_g = jnp.pad(chan_g.reshape(N_DEV, -1), ((0, 0), (0, CHAN_BLOCK_ROWS * D - N_CHAN_ROWS * LANES)))
        grads["chan"] = chan_g.reshape(N_DEV * CHAN_BLOCK_ROWS, D)
        grads["gates"] = jnp.concatenate([grads[n].reshape(-1, D) for n in GATE_BLOCKS], axis=0).astype(BF16)
        return pair_exchange_at_once("early", early_sections, grads, PAIR_EARLY_ID)

    def finish(group, sections, after, name):
        send_sems, recv_sems, parts, zones, _ = in_flight[group]
        parts, far = _chip_exchange_wait(sections, send_sems, recv_sems, parts, zones, after,
                                         "grad_chip_exchange_wait_" + name)
        return dict(zip((n for n, _, _ in sections),
                        _grad_finish(sections, parts, far, chip, "grad_finish_" + name)))

    summed = {}

    def reduce_late(grads):
        in_flight["pair_late"] = _pair_exchange_start(
            late_sections, [grads[n] for n, _, _ in late_sections], PAIR_LATE_ID,
            "grad_pair_exchange_start_late")
        summed.update(finish("first", first_sections, in_flight["pair_late"][-1], "first"))
        summed.update(finish("early", early_sections, summed["w_ff2"], "early"))
        return pair_sum_and_send("late", late_sections, summed["gates"])

    loss_part, grad_x, grads = _local_step(x[0], loss_target[0], p, late_weights, late_weights_ready,
                                           reduce_first, reduce_early, reduce_late)

    flat = jnp.concatenate([grads[n].reshape(-1) for n, _ in REPLICATED if n not in GATE_BLOCKS]
                           + [loss_part.reshape(-1)])
    n_small = flat.shape[0]
    flat = jnp.pad(flat, (0, SMALL_ROWS * LANES - n_small)).reshape(SMALL_ROWS, LANES)
    *small_gather, small_started = _gather_start([flat, summed["gates"]], None, "small_grad_gather_start")

    g, delta, new_m, new_v = {}, {}, {}, {}

    def update(n, g2, shape2, after=None):
        d2, m2, v2 = _adamw(w[n].reshape(shape2), g2, m[n].reshape(shape2), v[n].reshape(shape2),
                            "adamw_" + n, after)
        g[n], delta[n], new_m[n], new_v[n] = (a.reshape(w[n].shape) for a in (g2, d2, m2, v2))

    for n in ROW_SHARDED:
        update(n, summed[n], summed[n].shape, small_started)
    for n, t in TRANSPOSED.items():
        if t in summed:
            update(n, summed[t].T, summed[t].shape[::-1], small_started)

    small_rows = [SMALL_ROWS, GATE_ROWS]
    _, small_zones = _gather_wait(
        *small_gather[:2], small_rows, range(2), *small_gather[2:],
        _after_all(list(delta.values()), "sharded_updates_done"), "small_grad_gather_wait")
    small_parts, gate_sum = _gather_pass_on(small_rows, small_zones, SMALL_PASS_ON_ID,
                                            "small_grad_gather_pass_on")
    small = _sum_devices(small_parts, SMALL_ROWS, "small_grad_sum").reshape(-1)
    loss = small[n_small - 1]

    small_params = []
    o = 0
    for n, shape2 in REPLICATED:
        if n in GATE_BLOCKS:
            k, rows = GATE_BLOCKS.index(n), gate_sum.shape[0] // len(GATE_BLOCKS)
            update(n, gate_sum[k * rows:(k + 1) * rows].reshape(shape2), shape2)
        else:
            size = shape2[0] * shape2[1]
            small_params.append((n, small[o:o + size].reshape(shape2), shape2))
            o += size
    chan_back = summed["chan"].reshape(-1)[:N_CHAN_ROWS * LANES].reshape(N_CHAN_ROWS, LANES)
    r0 = 0
    for n, rows in CHAN:
        small_params.append((n, chan_back[r0:r0 + rows], (rows, LANES)))
        r0 += rows
    results = _adamw_small([(w[n].reshape(s2), g2, m[n].reshape(s2), v[n].reshape(s2))
                            for n, g2, s2 in small_params], "adamw_vectors")
    for (n, g2, _), (d2, m2, v2) in zip(small_params, results):
        g[n], delta[n], new_m[n], new_v[n] = (a.reshape(w[n].shape) for a in (g2, d2, m2, v2))

    summed = finish("late", late_sections, _after_all(list(delta.values()), "updates_done"), "late")
    g_t = summed["w_in_t"]
    results = _adamw(w["w_in"][0].T, g_t, m["w_in"][0].T, v["w_in"][0].T, "adamw_w_in")
    g["w_in"], delta["w_in"], new_m["w_in"], new_v["w_in"] = (a.T[None] for a in (g_t, *results))

    return (loss, grad_x[None], *[g[n] for n in NAMES], *[delta[n] for n in NAMES],
            *[new_m[n] for n in NAMES], *[new_v[n] for n in NAMES])
```

```python
import math

import numpy as np
import jax
import jax.numpy as jnp
from jax import lax
from jax.experimental import pallas as pl
from jax.experimental.pallas import tpu as pltpu

F32 = jnp.float32
BF16 = jnp.bfloat16

T = 2048
D = 1024
D_ATT = 512
D_REC = 1024
D_FF = 4096
D_IN = 5632
N_HEADS = 8
DH = 64
GRID_W = 64
ROWS = T // GRID_W
WIN_H = 8
WIN_W = 16
KWIN = WIN_H * GRID_W
N_RPB_R = 2 * WIN_H - 1
N_RPB_C = 2 * WIN_W - 1
N_REC_BLOCKS = 16
REC_BLOCK = 64
CG = 128
N_CG = D_REC // CG
LRU_C = 8.0
EPS = 1e-6
N_DEV = 8
N_CHIPS = 4
LANES = 128

ADAM_LR = 0.001
ADAM_B1 = 0.9
ADAM_B2 = 0.999
ADAM_EPS = 1e-08
ADAM_WD = 0.01
ADAM_STEP = 10

MESH_AXES = ("x", "y", "c")
VMEM_LIMIT = 56 * 1024 * 1024

TILE = 512
DZ_ARRAYS = ((0, 3, 1), (3, 4, 2), (7, 4, 2))
N_DZ_TILES = D_IN // TILE


def _params(**kw):
    return pltpu.CompilerParams(vmem_limit_bytes=VMEM_LIMIT, **kw)


HG = 4
HQ = HG * GRID_W
HC = HG * DH


def _att_tables():
    rq = np.arange(GRID_W)
    kc = np.arange(KWIN) % GRID_W
    win_start = np.clip(rq - WIN_W // 2, 0, GRID_W - WIN_W)
    valid = (kc[None, :] >= win_start[:, None]) & (kc[None, :] < win_start[:, None] + WIN_W)
    same_head = (np.arange(HQ)[:, None] // GRID_W) == (np.arange(HC)[None, :] // DH)
    return valid.astype(np.float32), same_head.astype(np.float32)


def _pair_mask():
    half = np.arange(2 * DH) // DH
    return (half[:, None] == half[None, :]).astype(np.float32)


def _dup_table():
    return np.concatenate([np.eye(REC_BLOCK, dtype=np.float32)] * 2, axis=1)


def _sigmoid(x):
    return 0.5 * jnp.tanh(0.5 * x) + 0.5


def _softplus(x):
    return jnp.maximum(x, 0.0) + jnp.log(1.0 + jnp.exp(-jnp.abs(x)))


def _one_minus_square(log_a, a):
    x = 2.0 * log_a
    series = -x * (1.0 + x * (0.5 + x * (1.0 / 6.0)))
    return jnp.where(x > -0.02, series, 1.0 - a * a)


_GELU_C = math.sqrt(2.0 / math.pi)


def _gelu_and_grad(x):
    x2 = x * x
    inner = _GELU_C * (x + 0.044715 * x * x2)
    t = jnp.tanh(inner)
    g = 0.5 * x * (1.0 + t)
    dg = 0.5 * (1.0 + t) + 0.5 * x * (1.0 - t * t) * _GELU_C * (1.0 + 3.0 * 0.044715 * x2)
    return g, dg


def _dot(a, b):
    return jnp.dot(a, b, preferred_element_type=F32)


def _dot_nt(a, b):
    return lax.dot_general(a, b, (((1,), (1,)), ((), ())), preferred_element_type=F32)


def _dot_tn(a, b):
    return lax.dot_general(a, b, (((0,), (0,)), ((), ())), preferred_element_type=F32)


def _dot_exact(a, b):
    return jnp.dot(a, b, precision=lax.Precision.HIGHEST, preferred_element_type=F32)


def _shift_rows(x, s):
    n = x.shape[0]
    rows = lax.broadcasted_iota(jnp.int32, x.shape, 0)
    y = pltpu.roll(x, s % n, 0)
    if s > 0:
        return jnp.where(rows >= s, y, 0.0)
    return jnp.where(rows < n + s, y, 0.0)


def _rms_bwd(dh, xh, r, g):
    dxh = dh * g
    return r * (dxh - xh * jnp.mean(dxh * xh, axis=-1, keepdims=True))


def _matmul(a, b, mode, out_dtype, name, tm=512, tn=1024, tk=2048):
    if mode == "nn":
        (m, k), (k2, n) = a.shape, b.shape
    elif mode == "nt":
        (m, k), (n, k2) = a.shape, b.shape
    else:
        (k, m), (k2, n) = a.shape, b.shape
    assert k == k2
    tm, tn, tk = min(tm, m), min(tn, n), min(tk, k)
    assert m % tm == 0 and n % tn == 0 and k % tk == 0
    nk = k // tk
    dot = {"nn": _dot, "nt": _dot_nt, "tn": _dot_tn}[mode]

    def body(a_ref, b_ref, o_ref, acc):
        kk = pl.program_id(2)
        part = dot(a_ref[...].astype(BF16), b_ref[...].astype(BF16))
        if nk == 1:
            o_ref[...] = part.astype(out_dtype)
            return

        @pl.when(kk == 0)
        def _():
            acc[...] = part

        @pl.when(kk > 0)
        def _():
            acc[...] += part

        @pl.when(kk == nk - 1)
        def _():
            o_ref[...] = acc[...].astype(out_dtype)

    if mode == "tn":
        a_spec = pl.BlockSpec((tk, tm), lambda i, j, kk: (kk, i))
    else:
        a_spec = pl.BlockSpec((tm, tk), lambda i, j, kk: (i, kk))
    if mode == "nt":
        b_spec = pl.BlockSpec((tn, tk), lambda i, j, kk: (j, kk))
    else:
        b_spec = pl.BlockSpec((tk, tn), lambda i, j, kk: (kk, j))
    return pl.pallas_call(
        body, name=name,
        out_shape=jax.ShapeDtypeStruct((m, n), out_dtype),
        grid=(m // tm, n // tn, nk),
        in_specs=[a_spec, b_spec],
        out_specs=pl.BlockSpec((tm, tn), lambda i, j, kk: (i, j)),
        scratch_shapes=[pltpu.VMEM((tm, tn) if nk > 1 else (8, LANES), F32)],
        compiler_params=_params(dimension_semantics=("parallel", "parallel", "arbitrary")),
    )(a, b)


def _in_proj(x, g1, w_in_t, b_in, after):
    tm = 512

    def body(x_ref, g_ref, w_hbm, b_ref, after_ref, qkv_ref, uy_ref, gg_ref, h_ref, w):
        @pl.when(pl.program_id(0) == 0)
        def _():
            pltpu.sync_copy(w_hbm, w)

        xv = x_ref[...]
        r = lax.rsqrt(jnp.mean(xv * xv, axis=-1, keepdims=True) + EPS)
        h = ((xv * r) * g_ref[...]).astype(BF16)
        h_ref[...] = h
        row0 = 0
        for ref in (qkv_ref, uy_ref, gg_ref):
            for c0 in range(0, ref.shape[1], TILE):
                z = _dot_nt(h, w[row0:row0 + TILE, :]) + b_ref[:, row0:row0 + TILE]
                ref[:, c0:c0 + TILE] = z.astype(ref.dtype)
                row0 += TILE

    tok = lambda width: pl.BlockSpec((tm, width), lambda i: (i, 0))
    return pl.pallas_call(
        body, name="in_proj",
        out_shape=(jax.ShapeDtypeStruct((T, 3 * D_ATT), BF16),
                   jax.ShapeDtypeStruct((T, 2 * D_REC), F32),
                   jax.ShapeDtypeStruct((T, 2 * D), F32),
                   jax.ShapeDtypeStruct((T, D), BF16)),
        grid=(T // tm,),
        in_specs=[tok(D), pl.BlockSpec((1, D), lambda i: (0, 0)), pl.BlockSpec(memory_space=pl.ANY),
                  pl.BlockSpec((1, D_IN), lambda i: (0, 0)), pl.BlockSpec(memory_space=pl.ANY)],
        out_specs=(tok(3 * D_ATT), tok(2 * D_REC), tok(2 * D), tok(D)),
        scratch_shapes=[pltpu.VMEM((D_IN, D), BF16)],
        compiler_params=_params(dimension_semantics=("arbitrary",)),
    )(x, g1, w_in_t, b_in, after)


def _dz_specs(rows, tile_of, row_of):
    def spec(off, n, per_plane):
        def index(*ids):
            t = jnp.clip(tile_of(*ids) - off, 0, n - 1)
            return (t // per_plane, row_of(*ids), t % per_plane)
        return pl.BlockSpec((1, rows, TILE), index)
    return [spec(off, n, per) for off, n, per in DZ_ARRAYS]


def _dh_norm1_bwd(dz, w_in_t, x, g1, dx1, after):
    tm = 512

    def body(dqkv_ref, duy_ref, dgg_ref, w_hbm, x_ref, g_ref, dx1_ref, after_ref, gx_ref, dg_ref, w):
        @pl.when(pl.program_id(0) == 0)
        def _():
            pltpu.sync_copy(w_hbm, w)
            dg_ref[...] = jnp.zeros_like(dg_ref)

        dh, row0 = None, 0
        for ref in (dqkv_ref, duy_ref, dgg_ref):
            for plane in range(ref.shape[0]):
                cols = ref.shape[2]
                part = _dot(ref[plane], w[row0:row0 + cols, :])
                dh = part if dh is None else dh + part
                row0 += cols
        xv = x_ref[...]
        r = lax.rsqrt(jnp.mean(xv * xv, axis=-1, keepdims=True) + EPS)
        xh = xv * r
        dg_ref[...] += jnp.sum(dh * xh, axis=0, keepdims=True)
        gx_ref[...] = dx1_ref[...] + _rms_bwd(dh, xh, r, g_ref[...])

    tok = pl.BlockSpec((tm, D), lambda i: (i, 0))
    vec = pl.BlockSpec((1, D), lambda i: (0, 0))
    planes = lambda a: pl.BlockSpec((a.shape[0], tm, a.shape[2]), lambda i: (0, i, 0))
    return pl.pallas_call(
        body, name="dh_norm1_bwd",
        out_shape=(jax.ShapeDtypeStruct((T, D), F32), jax.ShapeDtypeStruct((1, D), F32)),
        grid=(T // tm,),
        in_specs=[planes(a) for a in dz] + [pl.BlockSpec(memory_space=pl.ANY), tok, vec, tok,
                                            pl.BlockSpec(memory_space=pl.ANY)],
        out_specs=(tok, vec),
        scratch_shapes=[pltpu.VMEM((D_IN, D), BF16)],
        compiler_params=_params(dimension_semantics=("arbitrary",)),
    )(*dz, w_in_t, x, g1, dx1, after)


def _grad_w_in(dz, h):
    def body(*refs):
        seg_refs = refs[:3]
        h_ref, gw_ref, gb_ref = refs[3:]
        j = pl.program_id(0)

        for s, (off, n, _) in enumerate(DZ_ARRAYS):
            @pl.when((j >= off) & (j < off + n))
            def _(s=s):
                a = seg_refs[s][0]
                gw_ref[...] = _dot_tn(a, h_ref[...]).astype(BF16)
                gb_ref[...] = jnp.sum(a.astype(F32), axis=0, keepdims=True)

    return pl.pallas_call(
        body, name="grad_w_in",
        out_shape=(jax.ShapeDtypeStruct((D_IN, D), BF16), jax.ShapeDtypeStruct((1, D_IN), F32)),
        grid=(N_DZ_TILES,),
        in_specs=_dz_specs(T, lambda j: j, lambda j: 0) + [pl.BlockSpec((T, D), lambda j: (0, 0))],
        out_specs=(pl.BlockSpec((TILE, D), lambda j: (j, 0)), pl.BlockSpec((1, TILE), lambda j: (0, j))),
        compiler_params=_params(dimension_semantics=("parallel",)),
    )(*dz, h)


def _rpb_rows(rpb):
    padded = jnp.pad(rpb, ((0, 0), (0, 0), (0, GRID_W - N_RPB_C)))
    rows = [padded[:, WIN_H - 1 - oi: 2 * WIN_H - 1 - oi].reshape(N_HEADS // HG, HG, KWIN)
            for oi in range(WIN_H)]
    return jnp.stack(rows, axis=0)


SKEW = KWIN - (WIN_W - 1)


MASKED = -1e30


def _bias_tiles(rows_ref, valid, bias_s):
    for oi in range(WIN_H):
        for hh in range(HG):
            row = jnp.broadcast_to(rows_ref[oi, 0, hh:hh + 1, :], (GRID_W, KWIN))
            tile = pltpu.roll(row, SKEW, 1, stride=1, stride_axis=0)
            bias_s[oi, hh * GRID_W:(hh + 1) * GRID_W, :] = jnp.where(valid, tile, MASKED)


def _bias_tile_grads(gb_s, flip, out_ref):
    for oi in range(WIN_H):
        for hh in range(HG):
            g = _dot_exact(flip, gb_s[oi, hh * GRID_W:(hh + 1) * GRID_W, :])
            back = pltpu.roll(g, KWIN - (GRID_W - WIN_W), 1, stride=1, stride_axis=0)
            out_ref[0, oi, hh:hh + 1, :] = jnp.sum(back, axis=0, keepdims=True)


def _rpb_fold(row_grads):
    g = row_grads.transpose(1, 0, 2, 3).reshape(WIN_H, N_HEADS, WIN_H, GRID_W)
    g = g.transpose(0, 2, 1, 3)

    def body(g_ref, o_ref):
        for dr in range(N_RPB_R):
            terms = [g_ref[oi, i] for oi in range(WIN_H) for i in range(WIN_H) if i - oi + WIN_H - 1 == dr]
            acc = terms[0]
            for term in terms[1:]:
                acc = acc + term
            o_ref[dr] = acc

    out = pl.pallas_call(
        body, name="rpb_fold",
        out_shape=jax.ShapeDtypeStruct((N_RPB_R, N_HEADS, GRID_W), F32),
    )(g)
    return out.transpose(1, 0, 2)[:, :, :N_RPB_C]


ATT_GROUPS = N_HEADS // HG
ATT_UNROLL = 8


def _stacked(rows64, same_head):
    return jnp.where(same_head, jnp.concatenate([rows64] * HG, axis=0), jnp.zeros((), BF16))


def _own_heads(stacked):
    head = lax.broadcasted_iota(jnp.int32, (GRID_W, HC), 1) // DH
    out = stacked[:GRID_W]
    for h in range(1, HG):
        out = jnp.where(head == h, stacked[h * GRID_W:(h + 1) * GRID_W], out)
    return out


def _att_scores(q_ref, k_ref, bias_ref, same_head, r):
    rs = jnp.clip(r - WIN_H // 2, 0, ROWS - WIN_H)
    oi = r - rs
    q0 = pl.multiple_of(r * GRID_W, GRID_W)
    k0 = pl.multiple_of(rs * GRID_W, GRID_W)
    q2 = _stacked(q_ref[pl.ds(q0, GRID_W), :] * (DH ** -0.5), same_head)
    kw = k_ref[pl.ds(k0, KWIN), :]
    s = _dot_nt(q2, kw) + bias_ref[oi]
    e = jnp.exp(s - jnp.max(s, axis=-1, keepdims=True))
    return e, 1.0 / jnp.sum(e, axis=-1, keepdims=True), q2, kw, q0, k0, oi


def _att_specs():
    col = lambda off: pl.BlockSpec((T, HC), lambda g: (0, g + off * ATT_GROUPS))
    tables = [pl.BlockSpec((WIN_H, 1, HG, KWIN), lambda g: (0, g, 0, 0)),
              pl.BlockSpec((GRID_W, KWIN), lambda g: (0, 0)),
              pl.BlockSpec((HQ, HC), lambda g: (0, 0))]
    return col, tables, pltpu.VMEM((WIN_H, HQ, KWIN), F32)


def _att_fwd(qkv, bias_rows, after):
    valid_np, same_head_np = _att_tables()

    def body(q_ref, k_ref, v_ref, rows_ref, valid_ref, head_ref, after_ref, o_ref, bias_s):
        same_head = head_ref[...] > 0.5
        _bias_tiles(rows_ref, valid_ref[...] > 0.5, bias_s)

        def row(r, carry):
            e, rl, _, _, q0, k0, _ = _att_scores(q_ref, k_ref, bias_s, same_head, r)
            o2 = _dot((e * rl).astype(BF16), v_ref[pl.ds(k0, KWIN), :])
            o_ref[pl.ds(q0, GRID_W), :] = _own_heads(o2).astype(BF16)
            return carry

        lax.fori_loop(0, ROWS, row, 0, unroll=ATT_UNROLL)

    col, tables, tiles = _att_specs()
    return pl.pallas_call(
        body, name="att_fwd",
        out_shape=jax.ShapeDtypeStruct((T, D_ATT), BF16),
        grid=(ATT_GROUPS,),
        in_specs=[col(0), col(1), col(2)] + tables + [pl.BlockSpec(memory_space=pl.ANY)],
        out_specs=col(0),
        scratch_shapes=[tiles],
        compiler_params=_params(dimension_semantics=("parallel",)),
    )(qkv, qkv, qkv, bias_rows, jnp.asarray(valid_np), jnp.asarray(same_head_np), after)


def _att_bwd(qkv, bias_rows, datt, after):
    valid_np, same_head_np = _att_tables()

    def body(q_ref, k_ref, v_ref, do_ref, rows_ref, valid_ref, head_ref, flip_ref, after_ref,
             dqkv_ref, grows_ref, dk_acc, dv_acc, bias_s, gb_s):
        same_head = head_ref[...] > 0.5
        dk_acc[...] = jnp.zeros_like(dk_acc)
        dv_acc[...] = jnp.zeros_like(dv_acc)
        gb_s[...] = jnp.zeros_like(gb_s)
        _bias_tiles(rows_ref, valid_ref[...] > 0.5, bias_s)

        def row(r, carry):
            e, rl, q2, kw, q0, k0, oi = _att_scores(q_ref, k_ref, bias_s, same_head, r)
            do2 = _stacked(do_ref[pl.ds(q0, GRID_W), :], same_head)
            vw = v_ref[pl.ds(k0, KWIN), :]
            p = e * rl
            dp = _dot_nt(do2, vw)
            ds = p * (dp - jnp.sum(dp * p, axis=-1, keepdims=True))
            p16 = p.astype(BF16)
            ds16 = ds.astype(BF16)
            dv_acc[pl.ds(k0, KWIN), :] += _dot_tn(p16, do2)
            dk_acc[pl.ds(k0, KWIN), :] += _dot_tn(ds16, q2)
            dq2 = _dot(ds16, kw) * (DH ** -0.5)
            dqkv_ref[0, pl.ds(q0, GRID_W), :] = _own_heads(dq2).astype(BF16)
            gb_s[oi] += ds
            return carry

        lax.fori_loop(0, ROWS, row, 0, unroll=ATT_UNROLL)
        dqkv_ref[1] = dk_acc[...].astype(BF16)
        dqkv_ref[2] = dv_acc[...].astype(BF16)
        _bias_tile_grads(gb_s, flip_ref[...], grows_ref)

    col, tables, tiles = _att_specs()
    return pl.pallas_call(
        body, name="att_bwd",
        out_shape=(jax.ShapeDtypeStruct((3, T, D_ATT), BF16),
                   jax.ShapeDtypeStruct((ATT_GROUPS, WIN_H, HG, KWIN), F32)),
        grid=(ATT_GROUPS,),
        in_specs=[col(0), col(1), col(2), col(0)] + tables + [pl.BlockSpec((GRID_W, GRID_W), lambda g: (0, 0)),
                                                              pl.BlockSpec(memory_space=pl.ANY)],
        out_specs=(pl.BlockSpec((3, T, HC), lambda g: (0, 0, g)),
                   pl.BlockSpec((1, WIN_H, HG, KWIN), lambda g: (g, 0, 0, 0))),
        scratch_shapes=[pltpu.VMEM((T, HC), F32), pltpu.VMEM((T, HC), F32), tiles, tiles],
        compiler_params=_params(dimension_semantics=("parallel",)),
    )(qkv, qkv, qkv, datt, bias_rows, jnp.asarray(valid_np), jnp.asarray(same_head_np),
      jnp.asarray(np.eye(GRID_W, dtype=np.float32)[::-1].copy()), after)


def _conv_taps(up):
    return (_shift_rows(up, 2), _shift_rows(up, 1), up, _shift_rows(up, -1))


def _pair_block_diag(w_pair, dup, same_half):
    return jnp.where(same_half, _dot(w_pair.astype(BF16), dup), 0.0).astype(BF16)


def _gates(u, u16, wa, ba, wi, bi, lam):
    r = _sigmoid(_dot(u16, wa) + ba)
    ig = _sigmoid(_dot(u16, wi) + bi)
    sp = _softplus(-lam)
    log_a = (-LRU_C) * r * sp
    a = jnp.exp(log_a)
    mult2 = jnp.maximum(_one_minus_square(log_a, a), 0.0)
    return r, ig, sp, a, jnp.sqrt(mult2), mult2


SCAN_BLOCKS = 8


def _scans(jobs):
    c = jobs[0][0].shape[1]
    nblk = T // 8
    rows = lax.broadcasted_iota(jnp.int32, (8, c), 0)

    def block(a, b, reverse):
        for s in (1, 2, 4):
            if reverse:
                keep = rows < 8 - s
                a_s = jnp.where(keep, pltpu.roll(a, 8 - s, 0), 1.0)
                b_s = jnp.where(keep, pltpu.roll(b, 8 - s, 0), 0.0)
            else:
                keep = rows >= s
                a_s = jnp.where(keep, pltpu.roll(a, s, 0), 1.0)
                b_s = jnp.where(keep, pltpu.roll(b, s, 0), 0.0)
            b = a * b_s + b
            a = a * a_s
        return a, b

    def step(i, carry):
        out = []
        for (a_ref, b_ref, h_ref, reverse), h_prev in zip(jobs, carry):
            for u in range(SCAN_BLOCKS):
                blk = i * SCAN_BLOCKS + u
                if reverse:
                    blk = nblk - 1 - blk
                t0 = pl.multiple_of(blk * 8, 8)
                a, b = block(a_ref[pl.ds(t0, 8), :], b_ref[pl.ds(t0, 8), :], reverse)
                h = a * h_prev + b
                h_ref[pl.ds(t0, 8), :] = h
                h_prev = jnp.broadcast_to(h[0:1] if reverse else h[7:8], (8, c))
            out.append(h_prev)
        return tuple(out)

    lax.fori_loop(0, nblk // SCAN_BLOCKS, step, tuple(jnp.zeros((8, c), F32) for _ in jobs))


def _rec_specs():
    tok = lambda off: pl.BlockSpec((T, CG), lambda g: (0, g + off))
    per_ch = lambda rows: pl.BlockSpec((rows, CG), lambda g: (0, g))
    wspec = pl.BlockSpec((2, 1, CG, REC_BLOCK), lambda g: (0, g, 0, 0))
    const = lambda shape: pl.BlockSpec(shape, lambda g: (0, 0))
    return tok, per_ch, wspec, const


def _rec_fwd(uy, conv_w, conv_b, w_a, b_a, w_i, b_i, lam):
    tok, per_ch, wspec, const = _rec_specs()

    def body(up_ref, yb_ref, cw_ref, cb_ref, wa_ref, ba_ref, wi_ref, bi_ref, lam_ref, dup_ref, half_ref,
             hf_ref, hb_ref, yrec_ref, am_ref, bx_f, bx_b):
        dup = dup_ref[...]
        same_half = half_ref[...] > 0.5
        taps = _conv_taps(up_ref[...])
        u = cb_ref[...]
        for j in range(4):
            u = u + taps[j] * cw_ref[j:j + 1, :]
        u16 = u.astype(BF16)
        for d, bx_s in enumerate((bx_f, bx_b)):
            wa = _pair_block_diag(wa_ref[d, 0], dup, same_half)
            wi = _pair_block_diag(wi_ref[d, 0], dup, same_half)
            _, ig, _, a, mult, _ = _gates(u, u16, wa, ba_ref[d:d + 1, :], wi, bi_ref[d:d + 1, :],
                                       lam_ref[d:d + 1, :])
            am_ref[2 * d] = a
            am_ref[2 * d + 1] = mult
            bx_s[...] = mult * (ig * u)
        _scans([(am_ref.at[0], bx_f, hf_ref, False), (am_ref.at[2], bx_b, hb_ref, True)])
        gelu, _ = _gelu_and_grad(yb_ref[...])
        yrec_ref[...] = ((hf_ref[...] + hb_ref[...]) * gelu).astype(BF16)

    return pl.pallas_call(
        body, name="rec_fwd",
        out_shape=(jax.ShapeDtypeStruct((T, D_REC), F32), jax.ShapeDtypeStruct((T, D_REC), F32),
                   jax.ShapeDtypeStruct((T, D_REC), BF16), jax.ShapeDtypeStruct((4, T, D_REC), F32)),
        grid=(N_CG,),
        in_specs=[tok(0), tok(N_CG), per_ch(4), per_ch(1), wspec, per_ch(2), wspec, per_ch(2), per_ch(2),
                  const((REC_BLOCK, CG)), const((CG, CG))],
        out_specs=(tok(0), tok(0), tok(0), pl.BlockSpec((4, T, CG), lambda g: (0, 0, g))),
        scratch_shapes=[pltpu.VMEM((T, CG), F32)] * 2,
        compiler_params=_params(dimension_semantics=("parallel",)),
    )(uy, uy, conv_w, conv_b, w_a, b_a, w_i, b_i, lam,
      jnp.asarray(_dup_table(), BF16), jnp.asarray(_pair_mask()))


def _rec_bwd(uy, hf, hb, am, dyrec, conv_w, conv_b, w_a, b_a, w_i, b_i, lam, after):
    tok, per_ch, wspec, const = _rec_specs()

    def body(up_ref, yb_ref, hf_ref, hb_ref, am_ref, dy_ref, cw_ref, cb_ref, wa_ref, ba_ref, wi_ref, bi_ref,
             lam_ref, dup_ref, dupt_ref, half_ref, after_ref,
             duy_ref, dcw_ref, dcb_ref, dwa_ref, dba_ref, dwi_ref, dbi_ref, dlam_ref,
             a_s0, a_s1, dh_s, g_s0, g_s1):
        dup = dup_ref[...]
        dup_t = dupt_ref[...]
        same_half = half_ref[...] > 0.5
        taps = _conv_taps(up_ref[...])
        u = cb_ref[...]
        for j in range(4):
            u = u + taps[j] * cw_ref[j:j + 1, :]
        u16 = u.astype(BF16)
        gelu, dgelu = _gelu_and_grad(yb_ref[...])
        dy = dy_ref[...]
        duy_ref[1] = (dy * (hf_ref[...] + hb_ref[...]) * dgelu).astype(BF16)
        dh_s[...] = dy * gelu
        a_s0[...] = _shift_rows(am_ref[0], -1)
        a_s1[...] = _shift_rows(am_ref[2], 1)
        _scans([(a_s0, dh_s, g_s0, True), (a_s1, dh_s, g_s1, False)])
        du = jnp.zeros((T, CG), F32)
        for d, g_s in enumerate((g_s0, g_s1)):
            reverse = d == 1
            wa = _pair_block_diag(wa_ref[d, 0], dup, same_half)
            wi = _pair_block_diag(wi_ref[d, 0], dup, same_half)
            lam_d = lam_ref[d:d + 1, :]
            r = _sigmoid(_dot(u16, wa) + ba_ref[d:d + 1, :])
            ig = _sigmoid(_dot(u16, wi) + bi_ref[d:d + 1, :])
            sp = _softplus(-lam_d)
            a, mult = am_ref[2 * d], am_ref[2 * d + 1]
            mult2 = mult * mult
            g = g_s[...]
            h_prev = _shift_rows(hb_ref[...], -1) if reverse else _shift_rows(hf_ref[...], 1)
            da = g * h_prev
            dmult = g * (ig * u)
            dig = g * mult * u
            du = du + g * mult * ig
            dmult_dlog = jnp.where(mult2 > 0.0, -(a * a) * lax.rsqrt(mult2), 0.0)
            dlog_a = da * a + dmult * dmult_dlog
            dr = dlog_a * ((-LRU_C) * sp)
            dsp = jnp.sum(dlog_a * ((-LRU_C) * r), axis=0, keepdims=True)
            dlam_ref[d:d + 1, :] = dsp * (-_sigmoid(-lam_d))
            dga = dr * r * (1.0 - r)
            dgi = dig * ig * (1.0 - ig)
            dga16 = dga.astype(BF16)
            dgi16 = dgi.astype(BF16)
            du = du + _dot_nt(dga16, wa) + _dot_nt(dgi16, wi)
            dwa_ref[d, 0] = _dot_exact(jnp.where(same_half, _dot_tn(u16, dga16), 0.0), dup_t)
            dwi_ref[d, 0] = _dot_exact(jnp.where(same_half, _dot_tn(u16, dgi16), 0.0), dup_t)
            dba_ref[d:d + 1, :] = jnp.sum(dga, axis=0, keepdims=True)
            dbi_ref[d:d + 1, :] = jnp.sum(dgi, axis=0, keepdims=True)
        dcb_ref[...] = jnp.sum(du, axis=0, keepdims=True)
        for j in range(4):
            dcw_ref[j:j + 1, :] = jnp.sum(du * taps[j], axis=0, keepdims=True)
        dup_in = (_shift_rows(du, -2) * cw_ref[0:1, :] + _shift_rows(du, -1) * cw_ref[1:2, :]
                  + du * cw_ref[2:3, :] + _shift_rows(du, 1) * cw_ref[3:4, :])
        duy_ref[0] = dup_in.astype(BF16)

    wshape = jax.ShapeDtypeStruct((2, N_CG, CG, REC_BLOCK), F32)
    vec = lambda rows: jax.ShapeDtypeStruct((rows, D_REC), F32)
    dup_np = _dup_table()
    return pl.pallas_call(
        body, name="rec_bwd",
        out_shape=(jax.ShapeDtypeStruct((2, T, D_REC), BF16),
                   vec(4), vec(1), wshape, vec(2), wshape, vec(2), vec(2)),
        grid=(N_CG,),
        in_specs=[tok(0), tok(N_CG), tok(0), tok(0), pl.BlockSpec((4, T, CG), lambda g: (0, 0, g)), tok(0),
                  per_ch(4), per_ch(1), wspec, per_ch(2), wspec, per_ch(2), per_ch(2),
                  const((REC_BLOCK, CG)), const((CG, REC_BLOCK)), const((CG, CG)),
                  pl.BlockSpec(memory_space=pl.ANY)],
        out_specs=(pl.BlockSpec((2, T, CG), lambda g: (0, 0, g)),
                   per_ch(4), per_ch(1), wspec, per_ch(2), wspec, per_ch(2), per_ch(2)),
        scratch_shapes=[pltpu.VMEM((T, CG), F32)] * 5,
        compiler_params=_params(dimension_semantics=("parallel",)),
    )(uy, uy, hf, hb, am, dyrec, conv_w, conv_b, w_a, b_a, w_i, b_i, lam,
      jnp.asarray(dup_np, BF16), jnp.asarray(dup_np.T.copy()), jnp.asarray(_pair_mask()), after)


TM_MIX = 256


def _mix_specs():
    tok = lambda width, blk=0: pl.BlockSpec((TM_MIX, width), lambda i: (i, blk))
    full = lambda shape: pl.BlockSpec(shape, lambda i: (0, 0))
    return tok, full


def _mix_fwd(x, att, yrec, gg, w_att_o_t, w_rec_o, w_out):
    tok, full = _mix_specs()

    def body(x_ref, att_ref, yr_ref, ga_ref, gr_ref, wao_ref, wro_ref, wo_ref, x1_ref, mixed_ref):
        y_att = _dot_nt(att_ref[...], wao_ref[...])
        y_rec = _dot(yr_ref[...], wro_ref[...])
        mixed = (_sigmoid(ga_ref[...]) * y_att + _sigmoid(gr_ref[...]) * y_rec).astype(BF16)
        mixed_ref[...] = mixed
        x1_ref[...] = x_ref[...] + _dot(mixed, wo_ref[...])

    return pl.pallas_call(
        body, name="mix_fwd",
        out_shape=(jax.ShapeDtypeStruct((T, D), F32), jax.ShapeDtypeStruct((T, D), BF16)),
        grid=(T // TM_MIX,),
        in_specs=[tok(D), tok(D_ATT), tok(D_REC), tok(D, 0), tok(D, 1),
                  full((D, D_ATT)), full((D_REC, D)), full((D, D))],
        out_specs=(tok(D), tok(D)),
        compiler_params=_params(dimension_semantics=("parallel",)),
    )(x, att, yrec, gg, gg, w_att_o_t, w_rec_o, w_out)


def _mix_bwd(dx1, att, yrec, gg, w_att_o_t, w_rec_o, w_out, after):
    tok, full = _mix_specs()

    def body(dx_ref, att_ref, yr_ref, ga_ref, gr_ref, wao_ref, wro_ref, wo_ref, after_ref,
             dgg_ref, dya_ref, dyr_ref, datt_ref, dyrp_ref):
        dmixed = _dot_nt(dx_ref[...].astype(BF16), wo_ref[...])
        y_att = _dot_nt(att_ref[...], wao_ref[...])
        y_rec = _dot(yr_ref[...], wro_ref[...])
        sa = _sigmoid(ga_ref[...])
        sr = _sigmoid(gr_ref[...])
        dgg_ref[0] = (dmixed * y_att * sa * (1.0 - sa)).astype(BF16)
        dgg_ref[1] = (dmixed * y_rec * sr * (1.0 - sr)).astype(BF16)
        dya = (dmixed * sa).astype(BF16)
        dyr = (dmixed * sr).astype(BF16)
        dya_ref[...] = dya
        dyr_ref[...] = dyr
        datt_ref[...] = _dot(dya, wao_ref[...]).astype(BF16)
        dyrp_ref[...] = _dot_nt(dyr, wro_ref[...])

    return pl.pallas_call(
        body, name="mix_bwd",
        out_shape=(jax.ShapeDtypeStruct((2, T, D), BF16),
                   jax.ShapeDtypeStruct((T, D), BF16), jax.ShapeDtypeStruct((T, D), BF16),
                   jax.ShapeDtypeStruct((T, D_ATT), BF16), jax.ShapeDtypeStruct((T, D_REC), F32)),
        grid=(T // TM_MIX,),
        in_specs=[tok(D), tok(D_ATT), tok(D_REC), tok(D, 0), tok(D, 1),
                  full((D, D_ATT)), full((D_REC, D)), full((D, D)), pl.BlockSpec(memory_space=pl.ANY)],
        out_specs=(pl.BlockSpec((2, TM_MIX, D), lambda i: (0, i, 0)),
                   tok(D), tok(D), tok(D_ATT), tok(D_REC)),
        compiler_params=_params(dimension_semantics=("parallel",)),
    )(dx1, att, yrec, gg, gg, w_att_o_t, w_rec_o, w_out, after)


TM_FFN = 256
FF_CHUNK = 1024


def _ffn_loss(x1, target, g2, gf, w_ff1_t, w_ff2):
    n_chunks = D_FF // FF_CHUNK

    def body(x1_ref, tg_ref, g2_ref, gf_ref, w1_hbm, w2_hbm,
             loss_ref, dx1_ref, h2_ref, act_ref, dpre_ref, dx2_ref, dg2_ref, dgf_ref,
             w1, w2, relu_s):
        i = pl.program_id(0)

        @pl.when(i == 0)
        def _():
            pltpu.sync_copy(w1_hbm, w1)
            pltpu.sync_copy(w2_hbm, w2)
            loss_ref[...] = jnp.zeros_like(loss_ref)
            dg2_ref[...] = jnp.zeros_like(dg2_ref)
            dgf_ref[...] = jnp.zeros_like(dgf_ref)

        x1v = x1_ref[...]
        r2 = lax.rsqrt(jnp.mean(x1v * x1v, axis=-1, keepdims=True) + EPS)
        xh2 = x1v * r2
        h2 = (xh2 * g2_ref[...]).astype(BF16)
        h2_ref[...] = h2
        x2 = x1v
        for c in range(n_chunks):
            ff = slice(c * FF_CHUNK, (c + 1) * FF_CHUNK)
            rl = jnp.maximum(_dot_nt(h2, w1[ff, :]), 0.0)
            relu_s[:, ff] = rl
            act = (rl * rl).astype(BF16)
            act_ref[:, ff] = act
            x2 = x2 + _dot(act, w2[ff, :])
        r3 = lax.rsqrt(jnp.mean(x2 * x2, axis=-1, keepdims=True) + EPS)
        xh3 = x2 * r3
        err = xh3 * gf_ref[...] - tg_ref[...]
        loss_ref[...] += 0.5 * jnp.sum(jnp.mean(err * err, axis=-1, keepdims=True))
        dy = err * (1.0 / D)
        dgf_ref[...] += jnp.sum(dy * xh3, axis=0, keepdims=True)
        dx2 = _rms_bwd(dy, xh3, r3, gf_ref[...])
        dx2_16 = dx2.astype(BF16)
        dx2_ref[...] = dx2_16
        dh2 = jnp.zeros((TM_FFN, D), F32)
        for c in range(n_chunks):
            ff = slice(c * FF_CHUNK, (c + 1) * FF_CHUNK)
            dpre = (_dot_nt(dx2_16, w2[ff, :]) * (2.0 * relu_s[:, ff])).astype(BF16)
            dpre_ref[:, ff] = dpre
            dh2 = dh2 + _dot(dpre, w1[ff, :])
        dg2_ref[...] += jnp.sum(dh2 * xh2, axis=0, keepdims=True)
        dx1_ref[...] = dx2 + _rms_bwd(dh2, xh2, r2, g2_ref[...])

    tok = lambda width: pl.BlockSpec((TM_FFN, width), lambda i: (i, 0))
    vec = pl.BlockSpec((1, D), lambda i: (0, 0))
    hbm = pl.BlockSpec(memory_space=pl.ANY)
    return pl.pallas_call(
        body, name="ffn_loss",
        out_shape=(jax.ShapeDtypeStruct((8, 128), F32), jax.ShapeDtypeStruct((T, D), F32),
                   jax.ShapeDtypeStruct((T, D), BF16), jax.ShapeDtypeStruct((T, D_FF), BF16),
                   jax.ShapeDtypeStruct((T, D_FF), BF16), jax.ShapeDtypeStruct((T, D), BF16),
                   jax.ShapeDtypeStruct((1, D), F32), jax.ShapeDtypeStruct((1, D), F32)),
        grid=(T // TM_FFN,),
        in_specs=[tok(D), tok(D), vec, vec, hbm, hbm],
        out_specs=(pl.BlockSpec((8, 128), lambda i: (0, 0)), tok(D), tok(D), tok(D_FF), tok(D_FF), tok(D),
                   vec, vec),
        scratch_shapes=[pltpu.VMEM((D_FF, D), BF16), pltpu.VMEM((D_FF, D), BF16),
                        pltpu.VMEM((TM_FFN, D_FF), F32)],
        compiler_params=_params(dimension_semantics=("arbitrary",)),
    )(x1, target, g2, gf, w_ff1_t, w_ff2)


def _local_step(x, target, p, late_weights, late_weights_ready, reduce_first, reduce_early, reduce_late):
    bias = _rpb_rows(p["rpb"])
    pairs = lambda w: w.reshape(2, N_CG, CG, REC_BLOCK)
    w_a, w_i = pairs(p["w_rg_a"]), pairs(p["w_rg_i"])
    rec_params = (p["conv_w"], p["conv_b"], w_a, p["b_rg_a"], w_i, p["b_rg_i"], p["lru_lambda"])

    qkv, uy, gg, h = _in_proj(x, p["ln1_g"], p["w_in_t"], p["b_in"], p["later_weights_started"])
    hf, hb, yrec, am = _rec_fwd(uy, *rec_params)
    att = _att_fwd(qkv, bias, late_weights(yrec, 0))
    p = {**p, **late_weights_ready(late_weights(att, 1), 0)}
    x1, mixed = _mix_fwd(x, att, yrec, gg, p["w_att_o_t"], p["w_rec_o"], p["w_out"])
    p = {**p, **late_weights_ready(x1, 1)}
    loss8, dx1, h2, act, dpre, dx2, g_ln2, g_lnf = _ffn_loss(
        x1, target, p["ln2_g"], p["lnf_g"], p["w_ff1_t"], p["w_ff2"])

    grads = {"ln2_g": g_ln2, "lnf_g": g_lnf,
             "w_ff1_t": _matmul(dpre, h2, "tn", BF16, "g_w_ff1"),
             "w_ff2": _matmul(act, dx2, "tn", BF16, "g_w_ff2")}
    dgg, dya, dyr, datt, dyrp = _mix_bwd(dx1, att, yrec, gg, p["w_att_o_t"], p["w_rec_o"], p["w_out"],
                                         reduce_first(grads, None))
    duy, g_cw, g_cb, g_wa, g_ba, g_wi, g_bi, g_lam = _rec_bwd(uy, hf, hb, am, dyrp, *rec_params,
                                                              reduce_first(None, dgg))
    blocks = lambda g: g.reshape(2, N_REC_BLOCKS, REC_BLOCK, REC_BLOCK)
    grads.update({
        "w_att_o_t": _matmul(dya, att, "tn", BF16, "g_w_att_o"),
        "conv_w": g_cw, "conv_b": g_cb, "w_rg_a": blocks(g_wa), "b_rg_a": g_ba,
        "w_rg_i": blocks(g_wi), "b_rg_i": g_bi, "lru_lambda": g_lam,
        "w_rec_o": _matmul(yrec, dyr, "tn", BF16, "g_w_rec_o"),
        "w_out": _matmul(mixed, dx1, "tn", BF16, "g_w_out"),
    })
    dqkv, gbias = _att_bwd(qkv, bias, datt, reduce_early(grads))
    dz = (dqkv, duy, dgg)
    g_w_in_t, g_b_in = _grad_w_in(dz, h)
    grads.update(w_in_t=g_w_in_t, b_in=g_b_in)
    grad_x, g_ln1 = _dh_norm1_bwd(dz, p["w_in_t"], x, p["ln1_g"], dx1, reduce_late(grads))
    grads.update(ln1_g=g_ln1, rpb=_rpb_fold(gbias))
    return loss8[0:1, 0:1], grad_x, grads


MESH_ID = pl.DeviceIdType.MESH
ANY = pl.BlockSpec(memory_space=pl.ANY)

CHAN_BLOCK_ROWS = 32
GATE_ROWS = 2 * 2 * N_REC_BLOCKS * REC_BLOCK * REC_BLOCK // (N_DEV * D)
SECTIONS = (("w_in_t", 704, D), ("w_rec_o", 128, D), ("w_out", 128, D), ("w_ff1_t", 512, D),
            ("w_ff2", 512, D), ("chan", CHAN_BLOCK_ROWS, D), ("w_att_o_t", 128, D_ATT),
            ("gates", GATE_ROWS, D))
N_SEC = len(SECTIONS)
N_CHAN_ROWS = 10
CHAN = (("conv_w", 4), ("b_rg_a", 2), ("b_rg_i", 2), ("lru_lambda", 2))


def _position():
    return lax.axis_index("x"), lax.axis_index("y"), lax.axis_index("c")


def _other_chips(x, y):
    return [(1 - x, y), (x, 1 - y), (1 - x, 1 - y)]


PASS_ON_IDS, PAIR_EARLY_ID, PAIR_LATE_ID, PAIR_FIRST_ID, SMALL_PASS_ON_ID = (1, 4), 2, 3, 5, 6


def _pair_handshake(x, y, c):
    barrier = pltpu.get_barrier_semaphore()
    pl.semaphore_signal(barrier, inc=1, device_id=(x, y, 1 - c), device_id_type=MESH_ID)
    pl.semaphore_wait(barrier, 1)


def _block_of(ref, dev, rows):
    return ref.at[pl.ds(pl.multiple_of(dev * rows, 16), rows)]


def _all_gather(shards, name, pieces):
    ns = len(shards)
    parts = [(s, j * (a.shape[0] // n), a.shape[0] // n)
             for s, (a, n) in enumerate(zip(shards, pieces)) for j in range(n)]
    np_ = len(parts)

    def body(*refs):
        x_refs, out_refs = refs[:ns], refs[ns:2 * ns]
        send_sems, recv_sems, local_sems = refs[2 * ns:]
        x, y, c = _position()
        me, sibling = (x, y, c), (x, y, 1 - c)
        x_nbr, y_nbr, diagonal = _other_chips(x, y)
        north = c == 1
        relay_from = (jnp.where(north, x_nbr[0], y_nbr[0]), jnp.where(north, x_nbr[1], y_nbr[1]))
        relay_to = (jnp.where(north, y_nbr[0], x_nbr[0]), jnp.where(north, y_nbr[1], x_nbr[1]))

        def rows(v, px, py, pc):
            s, r0, r = parts[v]
            start = (4 * px + 2 * py + pc) * shards[s].shape[0] + r0
            return out_refs[s].at[pl.ds(pl.multiple_of(start, 16), r)]

        def own(v):
            s, r0, r = parts[v]
            return x_refs[s].at[pl.ds(r0, r)]

        def copy(k, v, block, to, from_shard=False):
            return pltpu.make_async_remote_copy(
                src_ref=own(v) if from_shard else rows(v, *block), dst_ref=rows(v, *block),
                send_sem=send_sems.at[k * np_ + v], recv_sem=recv_sems.at[k * np_ + v],
                device_id=to, device_id_type=MESH_ID)

        sections = range(np_)
        mine = [pltpu.make_async_copy(own(v), rows(v, *me), local_sems.at[v]) for v in sections]
        sent = [copy(k, v, me, to, True) for v in sections
                for k, to in enumerate((sibling, (*x_nbr, c), (*y_nbr, c)))]
        for cp in mine + sent:
            cp.start()
        for s in sections:
            copy(1, s, (*x_nbr, c), me).wait_recv()
            copy(2, s, (*y_nbr, c), me).wait_recv()
            sent += [copy(3, s, (*relay_from, c), (*relay_to, c)),
                     copy(4, s, (*x_nbr, c), sibling), copy(5, s, (*y_nbr, c), sibling)]
            for cp in sent[-3:]:
                cp.start()
        for s in sections:
            copy(3, s, (*diagonal, c), me).wait_recv()
            sent.append(copy(6, s, (*diagonal, c), sibling))
            sent[-1].start()
        for s in sections:
            copy(0, s, sibling, me).wait_recv()
            for k, chip in ((4, x_nbr), (5, y_nbr), (6, diagonal)):
                copy(k, s, (*chip, 1 - c), me).wait_recv()
        for cp in sent:
            cp.wait_send()
        for cp in mine:
            cp.wait()

    return pl.pallas_call(
        body, name=name,
        out_shape=tuple(jax.ShapeDtypeStruct((N_DEV * s.shape[0], s.shape[1]), s.dtype) for s in shards),
        in_specs=[ANY] * ns,
        out_specs=(ANY,) * ns,
        scratch_shapes=[pltpu.SemaphoreType.DMA((7 * np_,)), pltpu.SemaphoreType.DMA((7 * np_,)),
                        pltpu.SemaphoreType.DMA((np_,))],
    )(*shards)


HBM = pl.BlockSpec(memory_space=pltpu.HBM)
SEM = pl.BlockSpec(memory_space=pltpu.SEMAPHORE)
EFFECT = pltpu.SideEffectType.DATAFLOW_SIDE_EFFECTING


def _in_hbm(a):
    return pltpu.with_memory_space_constraint(a, pltpu.HBM)


def _first_hop_copies(rows, which, x_refs, zones, send_sems, recv_sems):
    ns = len(rows)
    x, y, c = _position()
    targets = [(x, y, 1 - c)] + [(cx, cy, c) for cx, cy in _other_chips(x, y)]
    return [pltpu.make_async_remote_copy(
        src_ref=x_refs[i], dst_ref=_block_of(zones[i], 4 * x + 2 * y + c, rows[s]),
        send_sem=send_sems.at[k * ns + s], recv_sem=recv_sems.at[k * ns + s],
        device_id=to, device_id_type=MESH_ID)
        for k, to in enumerate(targets) for i, s in enumerate(which)]


def _after_all(arrays, name):
    def body(*refs):
        refs[-1][...] = jnp.zeros_like(refs[-1])

    return pl.pallas_call(
        body, name=name,
        out_shape=jax.ShapeDtypeStruct((8, LANES), F32),
        in_specs=[pl.BlockSpec(memory_space=pl.ANY)] * len(arrays),
        out_specs=pl.BlockSpec(memory_space=pltpu.VMEM),
    )(*arrays)


def _own_blocks_placed(shards, after, name):
    ns = len(shards)
    x, y, c = _position()
    me = jnp.reshape(4 * x + 2 * y + c, (1,)).astype(jnp.int32)
    tokens = [] if after is None else [after]

    def body(me_ref, *refs):
        for s in range(ns):
            refs[ns + len(tokens) + s][...] = refs[s][...]

    return pl.pallas_call(
        body, name=name,
        out_shape=tuple(jax.ShapeDtypeStruct((N_DEV * s.shape[0], s.shape[1]), s.dtype) for s in shards),
        grid_spec=pltpu.PrefetchScalarGridSpec(
            num_scalar_prefetch=1, grid=(1,),
            in_specs=[pl.BlockSpec(s.shape, lambda i, me: (0, 0)) for s in shards] + [ANY] * len(tokens),
            out_specs=tuple(pl.BlockSpec(s.shape, lambda i, me: (me[0], 0)) for s in shards)),
        compiler_params=_params(dimension_semantics=("arbitrary",)),
    )(me, *shards, *tokens)


def _gather_start(shards, after, name):
    ns = len(shards)
    zones = _own_blocks_placed(shards, after, name + "_own_blocks")

    def body(*refs):
        for cp in _first_hop_copies([s.shape[0] for s in shards], range(ns), refs[:ns], refs[ns:2 * ns],
                                    refs[2 * ns], refs[2 * ns + 1]):
            cp.start()
        refs[-1][...] = jnp.zeros_like(refs[-1])

    out = pl.pallas_call(
        body, name=name,
        out_shape=(pltpu.SemaphoreType.DMA((4 * ns,)), pltpu.SemaphoreType.DMA((4 * ns,)),
                   *[pltpu.HBM(a.shape, a.dtype) for a in (*shards, *zones)],
                   jax.ShapeDtypeStruct((8, LANES), F32)),
        in_specs=[HBM] * (2 * ns),
        out_specs=(SEM, SEM, *[HBM] * (2 * ns), pl.BlockSpec(memory_space=pltpu.VMEM)),
        input_output_aliases={i: 2 + i for i in range(2 * ns)},
        compiler_params=pltpu.CompilerParams(has_side_effects=EFFECT),
    )(*[_in_hbm(a) for a in shards], *[_in_hbm(a) for a in zones])
    return out[0], out[1], out[2:2 + ns], out[2 + ns:2 + 2 * ns], out[-1]


def _gather_wait(send_sems, recv_sems, rows, which, shards, zones, after, name):
    ns = len(shards)

    def body(*refs):
        for cp in _first_hop_copies(rows, which, refs[:ns], refs[ns:2 * ns], refs[2 * ns], refs[2 * ns + 1]):
            cp.wait_send()
            cp.wait_recv()

    out = pl.pallas_call(
        body, name=name,
        out_shape=tuple(pltpu.HBM(a.shape, a.dtype) for a in (*shards, *zones)),
        in_specs=[HBM] * (2 * ns) + [SEM, SEM, ANY],
        out_specs=(HBM,) * (2 * ns),
        input_output_aliases={i: i for i in range(2 * ns)},
        compiler_params=pltpu.CompilerParams(has_side_effects=EFFECT),
    )(*shards, *zones, send_sems, recv_sems, after)
    return out[:ns], out[ns:]


def _pass_on_copies(rows, in_refs, out_refs, send_sems, recv_sems):
    ns = len(rows)
    x, y, c = _position()
    return [pltpu.make_async_remote_copy(
        src_ref=_block_of(in_refs[s], 4 * cx + 2 * cy + c, rows[s]),
        dst_ref=_block_of(out_refs[s], 4 * cx + 2 * cy + c, rows[s]),
        send_sem=send_sems.at[j * ns + s], recv_sem=recv_sems.at[j * ns + s],
        device_id=(x, y, 1 - c), device_id_type=MESH_ID)
        for j, (cx, cy) in enumerate(_other_chips(x, y)) for s in range(ns)]


def _pass_on_start(rows, zones, barrier_id, name):
    ns = len(zones)

    def body(*refs):
        _pair_handshake(*_position())
        for cp in _pass_on_copies(rows, refs[:ns], refs[:ns], refs[ns], refs[ns + 1]):
            cp.start()
        refs[-1][...] = jnp.zeros_like(refs[-1])

    out = pl.pallas_call(
        body, name=name,
        out_shape=(pltpu.SemaphoreType.DMA((3 * ns,)), pltpu.SemaphoreType.DMA((3 * ns,)),
                   *[pltpu.HBM(z.shape, z.dtype) for z in zones], jax.ShapeDtypeStruct((8, LANES), F32)),
        in_specs=[HBM] * ns,
        out_specs=(SEM, SEM, *[HBM] * ns, pl.BlockSpec(memory_space=pltpu.VMEM)),
        input_output_aliases={i: 2 + i for i in range(ns)},
        compiler_params=pltpu.CompilerParams(has_side_effects=EFFECT, collective_id=barrier_id),
    )(*[_in_hbm(z) for z in zones])
    return out[0], out[1], out[2:2 + ns], out[-1]


def _pass_on_wait(rows, send_sems, recv_sems, zones, after, name):
    ns = len(zones)

    def body(*refs):
        for cp in _pass_on_copies(rows, refs[:ns], refs[:ns], refs[ns], refs[ns + 1]):
            cp.wait_send()
            cp.wait_recv()

    return pl.pallas_call(
        body, name=name,
        out_shape=tuple(pltpu.HBM(z.shape, z.dtype) for z in zones),
        in_specs=[HBM] * ns + [SEM, SEM, ANY],
        out_specs=(HBM,) * ns,
        input_output_aliases={i: i for i in range(ns)},
        compiler_params=pltpu.CompilerParams(has_side_effects=EFFECT),
    )(*zones, send_sems, recv_sems, after)


def _gather_pass_on(rows, zones, barrier_id, name):
    ns = len(zones)

    def body(*refs):
        _pair_handshake(*_position())
        copies = _pass_on_copies(rows, refs[:ns], refs[ns:2 * ns], *refs[2 * ns:])
        for cp in copies:
            cp.start()
        for cp in copies:
            cp.wait_recv()
        for cp in copies:
            cp.wait_send()

    return pl.pallas_call(
        body, name=name,
        out_shape=tuple(jax.ShapeDtypeStruct(z.shape, z.dtype) for z in zones),
        in_specs=[ANY] * ns, out_specs=(ANY,) * ns,
        input_output_aliases={i: i for i in range(ns)},
        scratch_shapes=[pltpu.SemaphoreType.DMA((3 * ns,)), pltpu.SemaphoreType.DMA((3 * ns,))],
        compiler_params=pltpu.CompilerParams(collective_id=barrier_id),
    )(*zones)


def _pair_copies(sections, g_refs, land, send_sems, recv_sems):
    ns = len(sections)
    x, y, c = _position()
    return [pltpu.make_async_remote_copy(
        src_ref=_block_of(g_refs[s], 2 * k + 1 - c, rows), dst_ref=land[s].at[k],
        send_sem=send_sems.at[k * ns + s], recv_sem=recv_sems.at[k * ns + s],
        device_id=(x, y, 1 - c), device_id_type=MESH_ID)
        for k in range(N_CHIPS) for s, (_, rows, _) in enumerate(sections)]


def _pair_exchange_start(sections, grads, barrier_id, name):
    ns = len(sections)

    def body(*refs):
        _pair_handshake(*_position())
        for cp in _pair_copies(sections, refs[:ns], refs[ns:2 * ns], refs[2 * ns], refs[2 * ns + 1]):
            cp.start()
        refs[-1][...] = jnp.zeros_like(refs[-1])

    zones = [lax.empty((N_CHIPS, rows, cols), BF16) for _, rows, cols in sections]
    n = N_CHIPS * ns
    out = pl.pallas_call(
        body, name=name,
        out_shape=(pltpu.SemaphoreType.DMA((n,)), pltpu.SemaphoreType.DMA((n,)),
                   *[pltpu.HBM(a.shape, a.dtype) for a in (*grads, *zones)],
                   jax.ShapeDtypeStruct((8, LANES), F32)),
        in_specs=[HBM] * (2 * ns),
        out_specs=(SEM, SEM, *[HBM] * (2 * ns), pl.BlockSpec(memory_space=pltpu.VMEM)),
        input_output_aliases={i: 2 + i for i in range(2 * ns)},
        compiler_params=pltpu.CompilerParams(has_side_effects=EFFECT, collective_id=barrier_id),
    )(*[_in_hbm(a) for a in grads], *[_in_hbm(a) for a in zones])
    return out[0], out[1], out[2:2 + ns], out[2 + ns:2 + 2 * ns], out[-1]


def _pair_exchange_wait(sections, send_sems, recv_sems, grads, zones, after, name):
    ns = len(sections)

    def body(*refs):
        for cp in _pair_copies(sections, refs[:ns], refs[ns:2 * ns], refs[2 * ns], refs[2 * ns + 1]):
            cp.wait_send()
            cp.wait_recv()

    out = pl.pallas_call(
        body, name=name,
        out_shape=tuple(pltpu.HBM(a.shape, a.dtype) for a in (*grads, *zones)),
        in_specs=[HBM] * (2 * ns) + [SEM, SEM, ANY],
        out_specs=(HBM,) * (2 * ns),
        input_output_aliases={i: i for i in range(2 * ns)},
        compiler_params=pltpu.CompilerParams(has_side_effects=EFFECT),
    )(*grads, *zones, send_sems, recv_sems, after)
    return out[:ns], out[ns:]


def _pair_add(sections, grads, got, core, name):
    ns = len(sections)

    def body(core_ref, *refs):
        g_refs, got_refs, p_refs = refs[:ns], refs[ns:2 * ns], refs[2 * ns:]
        for s in range(ns):
            p_refs[s][0] = (g_refs[s][...].astype(F32) + got_refs[s][0].astype(F32)).astype(BF16)

    slot = [pl.BlockSpec((1, rows, cols), lambda k, c: (k, 0, 0)) for _, rows, cols in sections]
    return pl.pallas_call(
        body, name=name,
        out_shape=tuple(jax.ShapeDtypeStruct((N_CHIPS, rows, cols), BF16) for _, rows, cols in sections),
        grid_spec=pltpu.PrefetchScalarGridSpec(
            num_scalar_prefetch=1, grid=(N_CHIPS,),
            in_specs=[pl.BlockSpec((rows, cols), lambda k, c: (2 * k + c[0], 0)) for _, rows, cols in sections]
            + slot,
            out_specs=tuple(slot)),
        compiler_params=_params(dimension_semantics=("parallel",)),
    )(core, *grads, *got)


def _chip_copies(sections, p_refs, land, send_sems, recv_sems):
    ns = len(sections)
    x, y, c = _position()
    return [pltpu.make_async_remote_copy(
        src_ref=p_refs[s].at[2 * cx + cy], dst_ref=land[s].at[j],
        send_sem=send_sems.at[j * ns + s], recv_sem=recv_sems.at[j * ns + s],
        device_id=(cx, cy, c), device_id_type=MESH_ID)
        for j, (cx, cy) in enumerate(_other_chips(x, y)) for s in range(ns)]


def _chip_exchange(sections, parts, name):
    ns = len(sections)

    def body(*refs):
        copies = _chip_copies(sections, refs[:ns], refs[ns:2 * ns], *refs[2 * ns:])
        for cp in copies:
            cp.start()
        for cp in copies:
            cp.wait_recv()
        for cp in copies:
            cp.wait_send()

    n = 3 * ns
    return pl.pallas_call(
        body, name=name,
        out_shape=tuple(jax.ShapeDtypeStruct((3, rows, cols), BF16) for _, rows, cols in sections),
        in_specs=[ANY] * ns, out_specs=(ANY,) * ns,
        scratch_shapes=[pltpu.SemaphoreType.DMA((n,)), pltpu.SemaphoreType.DMA((n,))],
    )(*parts)


def _chip_exchange_start(sections, parts, name):
    ns = len(sections)

    def body(*refs):
        p_refs, land = refs[:ns], refs[ns:2 * ns]
        send_sems, recv_sems = refs[2 * ns], refs[2 * ns + 1]
        token = refs[-1]
        for cp in _chip_copies(sections, p_refs, land, send_sems, recv_sems):
            cp.start()
        token[...] = jnp.zeros_like(token)

    zones = [lax.empty((3, rows, cols), BF16) for _, rows, cols in sections]
    out = pl.pallas_call(
        body, name=name,
        out_shape=(pltpu.SemaphoreType.DMA((3 * ns,)), pltpu.SemaphoreType.DMA((3 * ns,)),
                   *[pltpu.HBM(a.shape, a.dtype) for a in parts], *[pltpu.HBM(a.shape, a.dtype) for a in zones],
                   jax.ShapeDtypeStruct((8, LANES), F32)),
        in_specs=[HBM] * (2 * ns),
        out_specs=(SEM, SEM, *[HBM] * (2 * ns), pl.BlockSpec(memory_space=pltpu.VMEM)),
        input_output_aliases={i: 2 + i for i in range(2 * ns)},
        compiler_params=pltpu.CompilerParams(has_side_effects=EFFECT),
    )(*[_in_hbm(a) for a in parts], *[_in_hbm(a) for a in zones])
    return out[0], out[1], out[2:2 + ns], out[2 + ns:2 + 2 * ns], out[-1]


def _chip_exchange_wait(sections, send_sems, recv_sems, parts, zones, after, name):
    ns = len(sections)

    def body(*refs):
        p_refs, land = refs[:ns], refs[ns:2 * ns]
        for cp in _chip_copies(sections, p_refs, land, refs[2 * ns], refs[2 * ns + 1]):
            cp.wait_send()
            cp.wait_recv()

    out = pl.pallas_call(
        body, name=name,
        out_shape=tuple(pltpu.HBM(a.shape, a.dtype) for a in (*parts, *zones)),
        in_specs=[HBM] * (2 * ns) + [SEM, SEM, ANY],
        out_specs=(HBM,) * (2 * ns),
        input_output_aliases={i: i for i in range(2 * ns)},
        compiler_params=pltpu.CompilerParams(has_side_effects=EFFECT),
    )(*parts, *zones, send_sems, recv_sems, after)
    return out[:ns], out[ns:]


def _grad_finish(sections, parts, far, chip, name):
    ns = len(sections)

    def body(chip_ref, *refs):
        p_refs, b_refs, g_refs = refs[:ns], refs[ns:2 * ns], refs[2 * ns:]
        for s in range(ns):
            g = p_refs[s][0].astype(F32)
            for j in range(3):
                g = g + b_refs[s][j].astype(F32)
            g_refs[s][...] = g

    half = [(rows // 2, cols) for _, rows, cols in sections]
    return pl.pallas_call(
        body, name=name,
        out_shape=tuple(jax.ShapeDtypeStruct((rows, cols), F32) for _, rows, cols in sections),
        grid_spec=pltpu.PrefetchScalarGridSpec(
            num_scalar_prefetch=1, grid=(2,),
            in_specs=[pl.BlockSpec((1, r, c), lambda i, chip: (chip[0], i, 0)) for r, c in half]
            + [pl.BlockSpec((3, r, c), lambda i, chip: (0, i, 0)) for r, c in half],
            out_specs=tuple(pl.BlockSpec((r, c), lambda i, chip: (i, 0)) for r, c in half)),
        compiler_params=_params(dimension_semantics=("parallel",)),
    )(chip, *parts, *far)


def _sum_devices(parts, rows, name):
    cols = parts.shape[1]
    tr = rows // 2

    def body(*refs):
        s = refs[0][...].astype(F32)
        for d in range(1, N_DEV):
            s = s + refs[d][...].astype(F32)
        refs[N_DEV][...] = s

    return pl.pallas_call(
        body, name=name,
        out_shape=jax.ShapeDtypeStruct((rows, cols), F32),
        grid=(2,),
        in_specs=[pl.BlockSpec((tr, cols), lambda i, d=d: (2 * d + i, 0)) for d in range(N_DEV)],
        out_specs=pl.BlockSpec((tr, cols), lambda i: (i, 0)),
        compiler_params=_params(dimension_semantics=("parallel",)),
    )(*([parts] * N_DEV))


def _adamw_step(w_ref, g_ref, m_ref, v_ref, d_ref, nm_ref, nv_ref):
    c1 = 1.0 / (1.0 - ADAM_B1 ** ADAM_STEP)
    c2 = 1.0 / (1.0 - ADAM_B2 ** ADAM_STEP)
    gv = g_ref[...]
    nm = ADAM_B1 * m_ref[...] + (1.0 - ADAM_B1) * gv
    nv = ADAM_B2 * v_ref[...] + (1.0 - ADAM_B2) * (gv * gv)
    nm_ref[...] = nm
    nv_ref[...] = nv
    d_ref[...] = (-ADAM_LR) * ((nm * c1) / (jnp.sqrt(nv * c2) + ADAM_EPS) + ADAM_WD * w_ref[...])


def _adamw_small(params, name):
    n = len(params)

    def body(*refs):
        for k in range(n):
            _adamw_step(*refs[4 * k:4 * k + 4], *refs[4 * n + 3 * k:4 * n + 3 * k + 3])

    out = pl.pallas_call(
        body, name=name,
        out_shape=tuple(jax.ShapeDtypeStruct(p[0].shape, F32) for p in params for _ in range(3)),
    )(*[a for p in params for a in p])
    return [out[3 * k:3 * k + 3] for k in range(n)]


def _adamw(w, g, m, v, name, after=None):
    rows, cols = w.shape
    tr = rows
    while tr * cols * 4 > (1 << 20) and tr % 16 == 0:
        tr //= 2
    tokens = [] if after is None else [after]

    def body(*refs):
        _adamw_step(*refs[:4], *refs[4 + len(tokens):])

    spec = pl.BlockSpec((tr, cols), lambda i: (i, 0))
    shape = jax.ShapeDtypeStruct((rows, cols), F32)
    return pl.pallas_call(
        body, name=name,
        out_shape=(shape, shape, shape),
        grid=(rows // tr,),
        in_specs=[spec] * 4 + [ANY] * len(tokens), out_specs=(spec,) * 3,
        compiler_params=_params(dimension_semantics=("parallel",)),
    )(w, g, m, v, *tokens)


NAMES = ("ln1_g", "w_in", "b_in", "rpb", "w_att_o", "conv_w", "conv_b", "w_rg_a", "b_rg_a", "w_rg_i",
         "b_rg_i", "lru_lambda", "w_rec_o", "w_out", "ln2_g", "w_ff1", "w_ff2", "lnf_g")
TRANSPOSED = {"w_in": "w_in_t", "w_att_o": "w_att_o_t", "w_ff1": "w_ff1_t"}
ROW_SHARDED = ("w_rec_o", "w_out", "w_ff2")
REPLICATED = (("ln1_g", (1, D)), ("b_in", (1, D_IN)), ("rpb", (N_HEADS * N_RPB_R, N_RPB_C)),
              ("conv_b", (1, D_REC)), ("w_rg_a", (2 * N_REC_BLOCKS * REC_BLOCK, REC_BLOCK)),
              ("w_rg_i", (2 * N_REC_BLOCKS * REC_BLOCK, REC_BLOCK)), ("ln2_g", (1, D)), ("lnf_g", (1, D)))
GATE_BLOCKS = ("w_rg_a", "w_rg_i")
SMALL_ROWS = 112


def _chan_bits(vectors):
    chan = jnp.concatenate(vectors, axis=0)
    bits = lax.bitcast_convert_type(chan, BF16).reshape(-1)
    return jnp.pad(bits, (0, CHAN_BLOCK_ROWS * D - bits.shape[0])).reshape(CHAN_BLOCK_ROWS, D)


def _chan_from_bits(gathered):
    bits = gathered.reshape(N_DEV, CHAN_BLOCK_ROWS * D)[:, :2 * N_CHAN_ROWS * LANES]
    chan = lax.bitcast_convert_type(bits.reshape(N_DEV, N_CHAN_ROWS, LANES, 2), F32)
    return chan.transpose(1, 0, 2).reshape(N_CHAN_ROWS, D)


def kernel(x, ln1_g, w_in, b_in, rpb, w_att_o, conv_w, conv_b, w_rg_a, b_rg_a, w_rg_i, b_rg_i, lru_lambda, w_rec_o, w_out, ln2_g, w_ff1, w_ff2, lnf_g, loss_target, m_ln1_g, m_w_in, m_b_in, m_rpb, m_w_att_o, m_conv_w, m_conv_b, m_w_rg_a, m_b_rg_a, m_w_rg_i, m_b_rg_i, m_lru_lambda, m_w_rec_o, m_w_out, m_ln2_g, m_w_ff1, m_w_ff2, m_lnf_g, v_ln1_g, v_w_in, v_b_in, v_rpb, v_w_att_o, v_conv_w, v_conv_b, v_w_rg_a, v_b_rg_a, v_w_rg_i, v_b_rg_i, v_lru_lambda, v_w_rec_o, v_w_out, v_ln2_g, v_w_ff1, v_w_ff2, v_lnf_g):
    w = dict(zip(NAMES, (ln1_g, w_in, b_in, rpb, w_att_o, conv_w, conv_b, w_rg_a, b_rg_a, w_rg_i,
                         b_rg_i, lru_lambda, w_rec_o, w_out, ln2_g, w_ff1, w_ff2, lnf_g)))
    m = dict(zip(NAMES, (m_ln1_g, m_w_in, m_b_in, m_rpb, m_w_att_o, m_conv_w, m_conv_b, m_w_rg_a,
                         m_b_rg_a, m_w_rg_i, m_b_rg_i, m_lru_lambda, m_w_rec_o, m_w_out, m_ln2_g,
                         m_w_ff1, m_w_ff2, m_lnf_g)))
    v = dict(zip(NAMES, (v_ln1_g, v_w_in, v_b_in, v_rpb, v_w_att_o, v_conv_w, v_conv_b, v_w_rg_a,
                         v_b_rg_a, v_w_rg_i, v_b_rg_i, v_lru_lambda, v_w_rec_o, v_w_out, v_ln2_g,
                         v_w_ff1, v_w_ff2, v_lnf_g)))
    xi, yi, ci = _position()

    shard = {t: w[n][0].T.astype(BF16) for n, t in TRANSPOSED.items()}
    shard.update({n: w[n][0].astype(BF16) for n in ROW_SHARDED})
    shard["chan"] = _chan_bits([w[n][0] for n, _ in CHAN])
    first, later = ("w_in_t", "chan"), ("w_rec_o", "w_out", "w_att_o_t", "w_ff1_t", "w_ff2")
    p = dict(zip(first, _all_gather([shard[n] for n in first], "weight_all_gather", (4, 1))))
    send_sems, recv_sems, sent, zones, token = _gather_start([shard[n] for n in later], p["w_in_t"],
                                                             "weight_gather_start")

    stages = (("w_rec_o", "w_out", "w_att_o_t"), ("w_ff1_t", "w_ff2"))
    passing = {}

    def late_weights(after, stage):
        which = [later.index(n) for n in stages[stage]]
        _, arrived = _gather_wait(
            send_sems, recv_sems, [shard[n].shape[0] for n in later], which, [sent[i] for i in which],
            [zones[i] for i in which], after, "weight_gather_wait_%d" % stage)
        passing[stage] = _pass_on_start(
            [shard[n].shape[0] for n in stages[stage]], arrived,
            PASS_ON_IDS[stage], "weight_pass_on_start_%d" % stage)
        return passing[stage][-1]

    def late_weights_ready(after, stage):
        pass_send_sems, pass_recv_sems, pass_zones, _ = passing[stage]
        return dict(zip(stages[stage], _pass_on_wait(
            [shard[n].shape[0] for n in stages[stage]], pass_send_sems, pass_recv_sems, pass_zones, after,
            "weight_pass_on_wait_%d" % stage)))

    chan = _chan_from_bits(p.pop("chan"))
    r0 = 0
    for n, rows in CHAN:
        p[n] = chan[r0:r0 + rows]
        r0 += rows
    p.update(ln1_g=w["ln1_g"], b_in=w["b_in"], later_weights_started=token, rpb=w["rpb"][0], conv_b=w["conv_b"],
             w_rg_a=w["w_rg_a"][0], w_rg_i=w["w_rg_i"][0], ln2_g=w["ln2_g"],
             lnf_g=w["lnf_g"].reshape(1, D))

    core = jnp.reshape(ci, (1,)).astype(jnp.int32)
    chip = jnp.reshape(2 * xi + yi, (1,)).astype(jnp.int32)
    first_sections = tuple(s for s in SECTIONS if s[0] in ("w_ff1_t", "w_ff2"))
    late_sections = SECTIONS[:1]
    early_sections = tuple(s for s in SECTIONS[1:] if s not in first_sections)
    in_flight = {}

    def pair_sum_and_send(group, sections, after):
        send_sems, recv_sems, sect, zones, _ = in_flight["pair_" + group]
        sect, got = _pair_exchange_wait(sections, send_sems, recv_sems, sect, zones, after,
                                        "grad_pair_exchange_wait_" + group)
        parts = _pair_add(sections, sect, got, core, "grad_pair_add_" + group)
        in_flight[group] = _chip_exchange_start(sections, parts, "grad_chip_exchange_start_" + group)
        return in_flight[group][-1]

    def pair_exchange_at_once(group, sections, grads, barrier_id):
        in_flight["pair_" + group] = _pair_exchange_start(
            sections, [grads[n] for n, _, _ in sections], barrier_id, "grad_pair_exchange_start_" + group)
        return pair_sum_and_send(group, sections, in_flight["pair_" + group][-1])

    def reduce_first(grads, after):
        if grads is None:
            return pair_sum_and_send("first", first_sections, after)
        in_flight["pair_first"] = _pair_exchange_start(
            first_sections, [grads[n] for n, _, _ in first_sections], PAIR_FIRST_ID,
            "grad_pair_exchange_start_first")
        return in_flight["pair_first"][-1]

    def reduce_early(grads):
        chan_g = jnp.concatenate([grads[n] for n, _ in CHAN], axis=0)
        chan_g = chan_g.reshape(N_CHAN_ROWS, N_DEV, LANES).transpose(1, 0, 2).astype(BF16)
        chan_g = jnp.pad(chan_g.reshape(N_DEV, -1), ((0, 0), (0, CHAN_BLOCK_ROWS * D - N_CHAN_ROWS * LANES)))
        grads["chan"] = chan_g.reshape(N_DEV * CHAN_BLOCK_ROWS, D)
        grads["gates"] = jnp.concatenate([grads[n].reshape(-1, D) for n in GATE_BLOCKS], axis=0).astype(BF16)
        return pair_exchange_at_once("early", early_sections, grads, PAIR_EARLY_ID)

    def finish(group, sections, after, name):
        send_sems, recv_sems, parts, zones, _ = in_flight[group]
        parts, far = _chip_exchange_wait(sections, send_sems, recv_sems, parts, zones, after,
                                         "grad_chip_exchange_wait_" + name)
        return dict(zip((n for n, _, _ in sections),
                        _grad_finish(sections, parts, far, chip, "grad_finish_" + name)))

    summed = {}

    def reduce_late(grads):
        in_flight["pair_late"] = _pair_exchange_start(
            late_sections, [grads[n] for n, _, _ in late_sections], PAIR_LATE_ID,
            "grad_pair_exchange_start_late")
        summed.update(finish("first", first_sections, in_flight["pair_late"][-1], "first"))
        summed.update(finish("early", early_sections, summed["w_ff2"], "early"))
        return pair_sum_and_send("late", late_sections, summed["gates"])

    loss_part, grad_x, grads = _local_step(x[0], loss_target[0], p, late_weights, late_weights_ready,
                                           reduce_first, reduce_early, reduce_late)

    flat = jnp.concatenate([grads[n].reshape(-1) for n, _ in REPLICATED if n not in GATE_BLOCKS]
                           + [loss_part.reshape(-1)])
    n_small = flat.shape[0]
    flat = jnp.pad(flat, (0, SMALL_ROWS * LANES - n_small)).reshape(SMALL_ROWS, LANES)
    *small_gather, small_started = _gather_start([flat, summed["gates"]], None, "small_grad_gather_start")

    g, delta, new_m, new_v = {}, {}, {}, {}

    def update(n, g2, shape2, after=None):
        d2, m2, v2 = _adamw(w[n].reshape(shape2), g2, m[n].reshape(shape2), v[n].reshape(shape2),
                            "adamw_" + n, after)
        g[n], delta[n], new_m[n], new_v[n] = (a.reshape(w[n].shape) for a in (g2, d2, m2, v2))

    for n in ROW_SHARDED:
        update(n, summed[n], summed[n].shape, small_started)
    for n, t in TRANSPOSED.items():
        if t in summed:
            update(n, summed[t].T, summed[t].shape[::-1], small_started)

    small_rows = [SMALL_ROWS, GATE_ROWS]
    _, small_zones = _gather_wait(
        *small_gather[:2], small_rows, range(2), *small_gather[2:],
        _after_all(list(delta.values()), "sharded_updates_done"), "small_grad_gather_wait")
    small_parts, gate_sum = _gather_pass_on(small_rows, small_zones, SMALL_PASS_ON_ID,
                                            "small_grad_gather_pass_on")
    small = _sum_devices(small_parts, SMALL_ROWS, "small_grad_sum").reshape(-1)
    loss = small[n_small - 1]

    small_params = []
    o = 0
    for n, shape2 in REPLICATED:
        if n in GATE_BLOCKS:
            k, rows = GATE_BLOCKS.index(n), gate_sum.shape[0] // len(GATE_BLOCKS)
            update(n, gate_sum[k * rows:(k + 1) * rows].reshape(shape2), shape2)
        else:
            size = shape2[0] * shape2[1]
            small_params.append((n, small[o:o + size].reshape(shape2), shape2))
            o += size
    chan_back = summed["chan"].reshape(-1)[:N_CHAN_ROWS * LANES].reshape(N_CHAN_ROWS, LANES)
    r0 = 0
    for n, rows in CHAN:
        small_params.append((n, chan_back[r0:r0 + rows], (rows, LANES)))
        r0 += rows
    results = _adamw_small([(w[n].reshape(s2), g2, m[n].reshape(s2), v[n].reshape(s2))
                            for n, g2, s2 in small_params], "adamw_vectors")
    for (n, g2, _), (d2, m2, v2) in zip(small_params, results):
        g[n], delta[n], new_m[n], new_v[n] = (a.reshape(w[n].shape) for a in (g2, d2, m2, v2))

    summed = finish("late", late_sections, _after_all(list(delta.values()), "updates_done"), "late")
    g_t = summed["w_in_t"]
    results = _adamw(w["w_in"][0].T, g_t, m["w_in"][0].T, v["w_in"][0].T, "adamw_w_in")
    g["w_in"], delta["w_in"], new_m["w_in"], new_v["w_in"] = (a.T[None] for a in (g_t, *results))

    return (loss, grad_x[None], *[g[n] for n in NAMES], *[delta[n] for n in NAMES],
            *[new_m[n] for n in NAMES], *[new_v[n] for n in NAMES])
```

```python
import math

import numpy as np
import jax
import jax.numpy as jnp
from jax import lax
from jax.experimental import pallas as pl
from jax.experimental.pallas import tpu as pltpu

F32 = jnp.float32
BF16 = jnp.bfloat16

T = 2048
D = 1024
D_ATT = 512
D_REC = 1024
D_FF = 4096
D_IN = 5632
N_HEADS = 8
DH = 64
GRID_W = 64
ROWS = T // GRID_W
WIN_H = 8
WIN_W = 16
KWIN = WIN_H * GRID_W
N_RPB_R = 2 * WIN_H - 1
N_RPB_C = 2 * WIN_W - 1
N_REC_BLOCKS = 16
REC_BLOCK = 64
CG = 128
N_CG = D_REC // CG
LRU_C = 8.0
EPS = 1e-6
N_DEV = 8
N_CHIPS = 4
LANES = 128

ADAM_LR = 0.001
ADAM_B1 = 0.9
ADAM_B2 = 0.999
ADAM_EPS = 1e-08
ADAM_WD = 0.01
ADAM_STEP = 10

MESH_AXES = ("x", "y", "c")
VMEM_LIMIT = 56 * 1024 * 1024

TILE = 512
DZ_ARRAYS = ((0, 3, 1), (3, 4, 2), (7, 4, 2))
N_DZ_TILES = D_IN // TILE


def _params(**kw):
    return pltpu.CompilerParams(vmem_limit_bytes=VMEM_LIMIT, **kw)


HG = 4
HQ = HG * GRID_W
HC = HG * DH


def _att_tables():
    rq = np.arange(GRID_W)
    kc = np.arange(KWIN) % GRID_W
    win_start = np.clip(rq - WIN_W // 2, 0, GRID_W - WIN_W)
    valid = (kc[None, :] >= win_start[:, None]) & (kc[None, :] < win_start[:, None] + WIN_W)
    same_head = (np.arange(HQ)[:, None] // GRID_W) == (np.arange(HC)[None, :] // DH)
    return valid.astype(np.float32), same_head.astype(np.float32)


def _pair_mask():
    half = np.arange(2 * DH) // DH
    return (half[:, None] == half[None, :]).astype(np.float32)


def _dup_table():
    return np.concatenate([np.eye(REC_BLOCK, dtype=np.float32)] * 2, axis=1)


def _sigmoid(x):
    return 0.5 * jnp.tanh(0.5 * x) + 0.5


def _softplus(x):
    return jnp.maximum(x, 0.0) + jnp.log(1.0 + jnp.exp(-jnp.abs(x)))


def _one_minus_square(log_a, a):
    x = 2.0 * log_a
    series = -x * (1.0 + x * (0.5 + x * (1.0 / 6.0)))
    return jnp.where(x > -0.02, series, 1.0 - a * a)


_GELU_C = math.sqrt(2.0 / math.pi)


def _gelu_and_grad(x):
    x2 = x * x
    inner = _GELU_C * (x + 0.044715 * x * x2)
    t = jnp.tanh(inner)
    g = 0.5 * x * (1.0 + t)
    dg = 0.5 * (1.0 + t) + 0.5 * x * (1.0 - t * t) * _GELU_C * (1.0 + 3.0 * 0.044715 * x2)
    return g, dg


def _dot(a, b):
    return jnp.dot(a, b, preferred_element_type=F32)


def _dot_nt(a, b):
    return lax.dot_general(a, b, (((1,), (1,)), ((), ())), preferred_element_type=F32)


def _dot_tn(a, b):
    return lax.dot_general(a, b, (((0,), (0,)), ((), ())), preferred_element_type=F32)


def _dot_exact(a, b):
    return jnp.dot(a, b, precision=lax.Precision.HIGHEST, preferred_element_type=F32)


def _shift_rows(x, s):
    n = x.shape[0]
    rows = lax.broadcasted_iota(jnp.int32, x.shape, 0)
    y = pltpu.roll(x, s % n, 0)
    if s > 0:
        return jnp.where(rows >= s, y, 0.0)
    return jnp.where(rows < n + s, y, 0.0)


def _rms_bwd(dh, xh, r, g):
    dxh = dh * g
    return r * (dxh - xh * jnp.mean(dxh * xh, axis=-1, keepdims=True))


def _matmul(a, b, mode, out_dtype, name, tm=512, tn=1024, tk=2048):
    if mode == "nn":
        (m, k), (k2, n) = a.shape, b.shape
    elif mode == "nt":
        (m, k), (n, k2) = a.shape, b.shape
    else:
        (k, m), (k2, n) = a.shape, b.shape
    assert k == k2
    tm, tn, tk = min(tm, m), min(tn, n), min(tk, k)
    assert m % tm == 0 and n % tn == 0 and k % tk == 0
    nk = k // tk
    dot = {"nn": _dot, "nt": _dot_nt, "tn": _dot_tn}[mode]

    def body(a_ref, b_ref, o_ref, acc):
        kk = pl.program_id(2)
        part = dot(a_ref[...].astype(BF16), b_ref[...].astype(BF16))
        if nk == 1:
            o_ref[...] = part.astype(out_dtype)
            return

        @pl.when(kk == 0)
        def _():
            acc[...] = part

        @pl.when(kk > 0)
        def _():
            acc[...] += part

        @pl.when(kk == nk - 1)
        def _():
            o_ref[...] = acc[...].astype(out_dtype)

    if mode == "tn":
        a_spec = pl.BlockSpec((tk, tm), lambda i, j, kk: (kk, i))
    else:
        a_spec = pl.BlockSpec((tm, tk), lambda i, j, kk: (i, kk))
    if mode == "nt":
        b_spec = pl.BlockSpec((tn, tk), lambda i, j, kk: (j, kk))
    else:
        b_spec = pl.BlockSpec((tk, tn), lambda i, j, kk: (kk, j))
    return pl.pallas_call(
        body, name=name,
        out_shape=jax.ShapeDtypeStruct((m, n), out_dtype),
        grid=(m // tm, n // tn, nk),
        in_specs=[a_spec, b_spec],
        out_specs=pl.BlockSpec((tm, tn), lambda i, j, kk: (i, j)),
        scratch_shapes=[pltpu.VMEM((tm, tn) if nk > 1 else (8, LANES), F32)],
        compiler_params=_params(dimension_semantics=("parallel", "parallel", "arbitrary")),
    )(a, b)


def _in_proj(x, g1, w_in_t, b_in, after):
    tm = 512

    def body(x_ref, g_ref, w_hbm, b_ref, after_ref, qkv_ref, uy_ref, gg_ref, h_ref, w):
        @pl.when(pl.program_id(0) == 0)
        def _():
            pltpu.sync_copy(w_hbm, w)

        xv = x_ref[...]
        r = lax.rsqrt(jnp.mean(xv * xv, axis=-1, keepdims=True) + EPS)
        h = ((xv * r) * g_ref[...]).astype(BF16)
        h_ref[...] = h
        row0 = 0
        for ref in (qkv_ref, uy_ref, gg_ref):
            for c0 in range(0, ref.shape[1], TILE):
                z = _dot_nt(h, w[row0:row0 + TILE, :]) + b_ref[:, row0:row0 + TILE]
                ref[:, c0:c0 + TILE] = z.astype(ref.dtype)
                row0 += TILE

    tok = lambda width: pl.BlockSpec((tm, width), lambda i: (i, 0))
    return pl.pallas_call(
        body, name="in_proj",
        out_shape=(jax.ShapeDtypeStruct((T, 3 * D_ATT), BF16),
                   jax.ShapeDtypeStruct((T, 2 * D_REC), F32),
                   jax.ShapeDtypeStruct((T, 2 * D), F32),
                   jax.ShapeDtypeStruct((T, D), BF16)),
        grid=(T // tm,),
        in_specs=[tok(D), pl.BlockSpec((1, D), lambda i: (0, 0)), pl.BlockSpec(memory_space=pl.ANY),
                  pl.BlockSpec((1, D_IN), lambda i: (0, 0)), pl.BlockSpec(memory_space=pl.ANY)],
        out_specs=(tok(3 * D_ATT), tok(2 * D_REC), tok(2 * D), tok(D)),
        scratch_shapes=[pltpu.VMEM((D_IN, D), BF16)],
        compiler_params=_params(dimension_semantics=("arbitrary",)),
    )(x, g1, w_in_t, b_in, after)


def _dz_specs(rows, tile_of, row_of):
    def spec(off, n, per_plane):
        def index(*ids):
            t = jnp.clip(tile_of(*ids) - off, 0, n - 1)
            return (t // per_plane, row_of(*ids), t % per_plane)
        return pl.BlockSpec((1, rows, TILE), index)
    return [spec(off, n, per) for off, n, per in DZ_ARRAYS]


def _dh_norm1_bwd(dz, w_in_t, x, g1, dx1, after):
    tm = 512

    def body(dqkv_ref, duy_ref, dgg_ref, w_hbm, x_ref, g_ref, dx1_ref, after_ref, gx_ref, dg_ref, w):
        @pl.when(pl.program_id(0) == 0)
        def _():
            pltpu.sync_copy(w_hbm, w)
            dg_ref[...] = jnp.zeros_like(dg_ref)

        dh, row0 = None, 0
        for ref in (dqkv_ref, duy_ref, dgg_ref):
            for plane in range(ref.shape[0]):
                cols = ref.shape[2]
                part = _dot(ref[plane], w[row0:row0 + cols, :])
                dh = part if dh is None else dh + part
                row0 += cols
        xv = x_ref[...]
        r = lax.rsqrt(jnp.mean(xv * xv, axis=-1, keepdims=True) + EPS)
        xh = xv * r
        dg_ref[...] += jnp.sum(dh * xh, axis=0, keepdims=True)
        gx_ref[...] = dx1_ref[...] + _rms_bwd(dh, xh, r, g_ref[...])

    tok = pl.BlockSpec((tm, D), lambda i: (i, 0))
    vec = pl.BlockSpec((1, D), lambda i: (0, 0))
    planes = lambda a: pl.BlockSpec((a.shape[0], tm, a.shape[2]), lambda i: (0, i, 0))
    return pl.pallas_call(
        body, name="dh_norm1_bwd",
        out_shape=(jax.ShapeDtypeStruct((T, D), F32), jax.ShapeDtypeStruct((1, D), F32)),
        grid=(T // tm,),
        in_specs=[planes(a) for a in dz] + [pl.BlockSpec(memory_space=pl.ANY), tok, vec, tok,
                                            pl.BlockSpec(memory_space=pl.ANY)],
        out_specs=(tok, vec),
        scratch_shapes=[pltpu.VMEM((D_IN, D), BF16)],
        compiler_params=_params(dimension_semantics=("arbitrary",)),
    )(*dz, w_in_t, x, g1, dx1, after)


def _grad_w_in(dz, h):
    def body(*refs):
        seg_refs = refs[:3]
        h_ref, gw_ref, gb_ref = refs[3:]
        j = pl.program_id(0)

        for s, (off, n, _) in enumerate(DZ_ARRAYS):
            @pl.when((j >= off) & (j < off + n))
            def _(s=s):
                a = seg_refs[s][0]
                gw_ref[...] = _dot_tn(a, h_ref[...]).astype(BF16)
                gb_ref[...] = jnp.sum(a.astype(F32), axis=0, keepdims=True)

    return pl.pallas_call(
        body, name="grad_w_in",
        out_shape=(jax.ShapeDtypeStruct((D_IN, D), BF16), jax.ShapeDtypeStruct((1, D_IN), F32)),
        grid=(N_DZ_TILES,),
        in_specs=_dz_specs(T, lambda j: j, lambda j: 0) + [pl.BlockSpec((T, D), lambda j: (0, 0))],
        out_specs=(pl.BlockSpec((TILE, D), lambda j: (j, 0)), pl.BlockSpec((1, TILE), lambda j: (0, j))),
        compiler_params=_params(dimension_semantics=("parallel",)),
    )(*dz, h)


def _rpb_rows(rpb):
    padded = jnp.pad(rpb, ((0, 0), (0, 0), (0, GRID_W - N_RPB_C)))
    rows = [padded[:, WIN_H - 1 - oi: 2 * WIN_H - 1 - oi].reshape(N_HEADS // HG, HG, KWIN)
            for oi in range(WIN_H)]
    return jnp.stack(rows, axis=0)


SKEW = KWIN - (WIN_W - 1)


MASKED = -1e30


def _bias_tiles(rows_ref, valid, bias_s):
    for oi in range(WIN_H):
        for hh in range(HG):
            row = jnp.broadcast_to(rows_ref[oi, 0, hh:hh + 1, :], (GRID_W, KWIN))
            tile = pltpu.roll(row, SKEW, 1, stride=1, stride_axis=0)
            bias_s[oi, hh * GRID_W:(hh + 1) * GRID_W, :] = jnp.where(valid, tile, MASKED)


def _bias_tile_grads(gb_s, flip, out_ref):
    for oi in range(WIN_H):
        for hh in range(HG):
            g = _dot_exact(flip, gb_s[oi, hh * GRID_W:(hh + 1) * GRID_W, :])
            back = pltpu.roll(g, KWIN - (GRID_W - WIN_W), 1, stride=1, stride_axis=0)
            out_ref[0, oi, hh:hh + 1, :] = jnp.sum(back, axis=0, keepdims=True)


def _rpb_fold(row_grads):
    g = row_grads.transpose(1, 0, 2, 3).reshape(WIN_H, N_HEADS, WIN_H, GRID_W)
    g = g.transpose(0, 2, 1, 3)

    def body(g_ref, o_ref):
        for dr in range(N_RPB_R):
            terms = [g_ref[oi, i] for oi in range(WIN_H) for i in range(WIN_H) if i - oi + WIN_H - 1 == dr]
            acc = terms[0]
            for term in terms[1:]:
                acc = acc + term
            o_ref[dr] = acc

    out = pl.pallas_call(
        body, name="rpb_fold",
        out_shape=jax.ShapeDtypeStruct((N_RPB_R, N_HEADS, GRID_W), F32),
    )(g)
    return out.transpose(1, 0, 2)[:, :, :N_RPB_C]


ATT_GROUPS = N_HEADS // HG
ATT_UNROLL = 8


def _stacked(rows64, same_head):
    return jnp.where(same_head, jnp.concatenate([rows64] * HG, axis=0), jnp.zeros((), BF16))


def _own_heads(stacked):
    head = lax.broadcasted_iota(jnp.int32, (GRID_W, HC), 1) // DH
    out = stacked[:GRID_W]
    for h in range(1, HG):
        out = jnp.where(head == h, stacked[h * GRID_W:(h + 1) * GRID_W], out)
    return out


def _att_scores(q_ref, k_ref, bias_ref, same_head, r):
    rs = jnp.clip(r - WIN_H // 2, 0, ROWS - WIN_H)
    oi = r - rs
    q0 = pl.multiple_of(r * GRID_W, GRID_W)
    k0 = pl.multiple_of(rs * GRID_W, GRID_W)
    q2 = _stacked(q_ref[pl.ds(q0, GRID_W), :] * (DH ** -0.5), same_head)
    kw = k_ref[pl.ds(k0, KWIN), :]
    s = _dot_nt(q2, kw) + bias_ref[oi]
    e = jnp.exp(s - jnp.max(s, axis=-1, keepdims=True))
    return e, 1.0 / jnp.sum(e, axis=-1, keepdims=True), q2, kw, q0, k0, oi


def _att_specs():
    col = lambda off: pl.BlockSpec((T, HC), lambda g: (0, g + off * ATT_GROUPS))
    tables = [pl.BlockSpec((WIN_H, 1, HG, KWIN), lambda g: (0, g, 0, 0)),
              pl.BlockSpec((GRID_W, KWIN), lambda g: (0, 0)),
              pl.BlockSpec((HQ, HC), lambda g: (0, 0))]
    return col, tables, pltpu.VMEM((WIN_H, HQ, KWIN), F32)


def _att_fwd(qkv, bias_rows, after):
    valid_np, same_head_np = _att_tables()

    def body(q_ref, k_ref, v_ref, rows_ref, valid_ref, head_ref, after_ref, o_ref, bias_s):
        same_head = head_ref[...] > 0.5
        _bias_tiles(rows_ref, valid_ref[...] > 0.5, bias_s)

        def row(r, carry):
            e, rl, _, _, q0, k0, _ = _att_scores(q_ref, k_ref, bias_s, same_head, r)
            o2 = _dot((e * rl).astype(BF16), v_ref[pl.ds(k0, KWIN), :])
            o_ref[pl.ds(q0, GRID_W), :] = _own_heads(o2).astype(BF16)
            return carry

        lax.fori_loop(0, ROWS, row, 0, unroll=ATT_UNROLL)

    col, tables, tiles = _att_specs()
    return pl.pallas_call(
        body, name="att_fwd",
        out_shape=jax.ShapeDtypeStruct((T, D_ATT), BF16),
        grid=(ATT_GROUPS,),
        in_specs=[col(0), col(1), col(2)] + tables + [pl.BlockSpec(memory_space=pl.ANY)],
        out_specs=col(0),
        scratch_shapes=[tiles],
        compiler_params=_params(dimension_semantics=("parallel",)),
    )(qkv, qkv, qkv, bias_rows, jnp.asarray(valid_np), jnp.asarray(same_head_np), after)


def _att_bwd(qkv, bias_rows, datt, after):
    valid_np, same_head_np = _att_tables()

    def body(q_ref, k_ref, v_ref, do_ref, rows_ref, valid_ref, head_ref, flip_ref, after_ref,
             dqkv_ref, grows_ref, dk_acc, dv_acc, bias_s, gb_s):
        same_head = head_ref[...] > 0.5
        dk_acc[...] = jnp.zeros_like(dk_acc)
        dv_acc[...] = jnp.zeros_like(dv_acc)
        gb_s[...] = jnp.zeros_like(gb_s)
        _bias_tiles(rows_ref, valid_ref[...] > 0.5, bias_s)

        def row(r, carry):
            e, rl, q2, kw, q0, k0, oi = _att_scores(q_ref, k_ref, bias_s, same_head, r)
            do2 = _stacked(do_ref[pl.ds(q0, GRID_W), :], same_head)
            vw = v_ref[pl.ds(k0, KWIN), :]
            p = e * rl
            dp = _dot_nt(do2, vw)
            ds = p * (dp - jnp.sum(dp * p, axis=-1, keepdims=True))
            p16 = p.astype(BF16)
            ds16 = ds.astype(BF16)
            dv_acc[pl.ds(k0, KWIN), :] += _dot_tn(p16, do2)
            dk_acc[pl.ds(k0, KWIN), :] += _dot_tn(ds16, q2)
            dq2 = _dot(ds16, kw) * (DH ** -0.5)
            dqkv_ref[0, pl.ds(q0, GRID_W), :] = _own_heads(dq2).astype(BF16)
            gb_s[oi] += ds
            return carry

        lax.fori_loop(0, ROWS, row, 0, unroll=ATT_UNROLL)
        dqkv_ref[1] = dk_acc[...].astype(BF16)
        dqkv_ref[2] = dv_acc[...].astype(BF16)
        _bias_tile_grads(gb_s, flip_ref[...], grows_ref)

    col, tables, tiles = _att_specs()
    return pl.pallas_call(
        body, name="att_bwd",
        out_shape=(jax.ShapeDtypeStruct((3, T, D_ATT), BF16),
                   jax.ShapeDtypeStruct((ATT_GROUPS, WIN_H, HG, KWIN), F32)),
        grid=(ATT_GROUPS,),
        in_specs=[col(0), col(1), col(2), col(0)] + tables + [pl.BlockSpec((GRID_W, GRID_W), lambda g: (0, 0)),
                                                              pl.BlockSpec(memory_space=pl.ANY)],
        out_specs=(pl.BlockSpec((3, T, HC), lambda g: (0, 0, g)),
                   pl.BlockSpec((1, WIN_H, HG, KWIN), lambda g: (g, 0, 0, 0))),
        scratch_shapes=[pltpu.VMEM((T, HC), F32), pltpu.VMEM((T, HC), F32), tiles, tiles],
        compiler_params=_params(dimension_semantics=("parallel",)),
    )(qkv, qkv, qkv, datt, bias_rows, jnp.asarray(valid_np), jnp.asarray(same_head_np),
      jnp.asarray(np.eye(GRID_W, dtype=np.float32)[::-1].copy()), after)


def _conv_taps(up):
    return (_shift_rows(up, 2), _shift_rows(up, 1), up, _shift_rows(up, -1))


def _pair_block_diag(w_pair, dup, same_half):
    return jnp.where(same_half, _dot(w_pair.astype(BF16), dup), 0.0).astype(BF16)


def _gates(u, u16, wa, ba, wi, bi, lam):
    r = _sigmoid(_dot(u16, wa) + ba)
    ig = _sigmoid(_dot(u16, wi) + bi)
    sp = _softplus(-lam)
    log_a = (-LRU_C) * r * sp
    a = jnp.exp(log_a)
    mult2 = jnp.maximum(_one_minus_square(log_a, a), 0.0)
    return r, ig, sp, a, jnp.sqrt(mult2), mult2


SCAN_BLOCKS = 8


def _scans(jobs):
    c = jobs[0][0].shape[1]
    nblk = T // 8
    rows = lax.broadcasted_iota(jnp.int32, (8, c), 0)

    def block(a, b, reverse):
        for s in (1, 2, 4):
            if reverse:
                keep = rows < 8 - s
                a_s = jnp.where(keep, pltpu.roll(a, 8 - s, 0), 1.0)
                b_s = jnp.where(keep, pltpu.roll(b, 8 - s, 0), 0.0)
            else:
                keep = rows >= s
                a_s = jnp.where(keep, pltpu.roll(a, s, 0), 1.0)
                b_s = jnp.where(keep, pltpu.roll(b, s, 0), 0.0)
            b = a * b_s + b
            a = a * a_s
        return a, b

    def step(i, carry):
        out = []
        for (a_ref, b_ref, h_ref, reverse), h_prev in zip(jobs, carry):
            for u in range(SCAN_BLOCKS):
                blk = i * SCAN_BLOCKS + u
                if reverse:
                    blk = nblk - 1 - blk
                t0 = pl.multiple_of(blk * 8, 8)
                a, b = block(a_ref[pl.ds(t0, 8), :], b_ref[pl.ds(t0, 8), :], reverse)
                h = a * h_prev + b
                h_ref[pl.ds(t0, 8), :] = h
                h_prev = jnp.broadcast_to(h[0:1] if reverse else h[7:8], (8, c))
            out.append(h_prev)
        return tuple(out)

    lax.fori_loop(0, nblk // SCAN_BLOCKS, step, tuple(jnp.zeros((8, c), F32) for _ in jobs))


def _rec_specs():
    tok = lambda off: pl.BlockSpec((T, CG), lambda g: (0, g + off))
    per_ch = lambda rows: pl.BlockSpec((rows, CG), lambda g: (0, g))
    wspec = pl.BlockSpec((2, 1, CG, REC_BLOCK), lambda g: (0, g, 0, 0))
    const = lambda shape: pl.BlockSpec(shape, lambda g: (0, 0))
    return tok, per_ch, wspec, const


def _rec_fwd(uy, conv_w, conv_b, w_a, b_a, w_i, b_i, lam):
    tok, per_ch, wspec, const = _rec_specs()

    def body(up_ref, yb_ref, cw_ref, cb_ref, wa_ref, ba_ref, wi_ref, bi_ref, lam_ref, dup_ref, half_ref,
             hf_ref, hb_ref, yrec_ref, am_ref, bx_f, bx_b):
        dup = dup_ref[...]
        same_half = half_ref[...] > 0.5
        taps = _conv_taps(up_ref[...])
        u = cb_ref[...]
        for j in range(4):
            u = u + taps[j] * cw_ref[j:j + 1, :]
        u16 = u.astype(BF16)
        for d, bx_s in enumerate((bx_f, bx_b)):
            wa = _pair_block_diag(wa_ref[d, 0], dup, same_half)
            wi = _pair_block_diag(wi_ref[d, 0], dup, same_half)
            _, ig, _, a, mult, _ = _gates(u, u16, wa, ba_ref[d:d + 1, :], wi, bi_ref[d:d + 1, :],
                                       lam_ref[d:d + 1, :])
            am_ref[2 * d] = a
            am_ref[2 * d + 1] = mult
            bx_s[...] = mult * (ig * u)
        _scans([(am_ref.at[0], bx_f, hf_ref, False), (am_ref.at[2], bx_b, hb_ref, True)])
        gelu, _ = _gelu_and_grad(yb_ref[...])
        yrec_ref[...] = ((hf_ref[...] + hb_ref[...]) * gelu).astype(BF16)

    return pl.pallas_call(
        body, name="rec_fwd",
        out_shape=(jax.ShapeDtypeStruct((T, D_REC), F32), jax.ShapeDtypeStruct((T, D_REC), F32),
                   jax.ShapeDtypeStruct((T, D_REC), BF16), jax.ShapeDtypeStruct((4, T, D_REC), F32)),
        grid=(N_CG,),
        in_specs=[tok(0), tok(N_CG), per_ch(4), per_ch(1), wspec, per_ch(2), wspec, per_ch(2), per_ch(2),
                  const((REC_BLOCK, CG)), const((CG, CG))],
        out_specs=(tok(0), tok(0), tok(0), pl.BlockSpec((4, T, CG), lambda g: (0, 0, g))),
        scratch_shapes=[pltpu.VMEM((T, CG), F32)] * 2,
        compiler_params=_params(dimension_semantics=("parallel",)),
    )(uy, uy, conv_w, conv_b, w_a, b_a, w_i, b_i, lam,
      jnp.asarray(_dup_table(), BF16), jnp.asarray(_pair_mask()))


def _rec_bwd(uy, hf, hb, am, dyrec, conv_w, conv_b, w_a, b_a, w_i, b_i, lam, after):
    tok, per_ch, wspec, const = _rec_specs()

    def body(up_ref, yb_ref, hf_ref, hb_ref, am_ref, dy_ref, cw_ref, cb_ref, wa_ref, ba_ref, wi_ref, bi_ref,
             lam_ref, dup_ref, dupt_ref, half_ref, after_ref,
             duy_ref, dcw_ref, dcb_ref, dwa_ref, dba_ref, dwi_ref, dbi_ref, dlam_ref,
             a_s0, a_s1, dh_s, g_s0, g_s1):
        dup = dup_ref[...]
        dup_t = dupt_ref[...]
        same_half = half_ref[...] > 0.5
        taps = _conv_taps(up_ref[...])
        u = cb_ref[...]
        for j in range(4):
            u = u + taps[j] * cw_ref[j:j + 1, :]
        u16 = u.astype(BF16)
        gelu, dgelu = _gelu_and_grad(yb_ref[...])
        dy = dy_ref[...]
        duy_ref[1] = (dy * (hf_ref[...] + hb_ref[...]) * dgelu).astype(BF16)
        dh_s[...] = dy * gelu
        a_s0[...] = _shift_rows(am_ref[0], -1)
        a_s1[...] = _shift_rows(am_ref[2], 1)
        _scans([(a_s0, dh_s, g_s0, True), (a_s1, dh_s, g_s1, False)])
        du = jnp.zeros((T, CG), F32)
        for d, g_s in enumerate((g_s0, g_s1)):
            reverse = d == 1
            wa = _pair_block_diag(wa_ref[d, 0], dup, same_half)
            wi = _pair_block_diag(wi_ref[d, 0], dup, same_half)
            lam_d = lam_ref[d:d + 1, :]
            r = _sigmoid(_dot(u16, wa) + ba_ref[d:d + 1, :])
            ig = _sigmoid(_dot(u16, wi) + bi_ref[d:d + 1, :])
            sp = _softplus(-lam_d)
            a, mult = am_ref[2 * d], am_ref[2 * d + 1]
            mult2 = mult * mult
            g = g_s[...]
            h_prev = _shift_rows(hb_ref[...], -1) if reverse else _shift_rows(hf_ref[...], 1)
            da = g * h_prev
            dmult = g * (ig * u)
            dig = g * mult * u
            du = du + g * mult * ig
            dmult_dlog = jnp.where(mult2 > 0.0, -(a * a) * lax.rsqrt(mult2), 0.0)
            dlog_a = da * a + dmult * dmult_dlog
            dr = dlog_a * ((-LRU_C) * sp)
            dsp = jnp.sum(dlog_a * ((-LRU_C) * r), axis=0, keepdims=True)
            dlam_ref[d:d + 1, :] = dsp * (-_sigmoid(-lam_d))
            dga = dr * r * (1.0 - r)
            dgi = dig * ig * (1.0 - ig)
            dga16 = dga.astype(BF16)
            dgi16 = dgi.astype(BF16)
            du = du + _dot_nt(dga16, wa) + _dot_nt(dgi16, wi)
            dwa_ref[d, 0] = _dot_exact(jnp.where(same_half, _dot_tn(u16, dga16), 0.0), dup_t)
            dwi_ref[d, 0] = _dot_exact(jnp.where(same_half, _dot_tn(u16, dgi16), 0.0), dup_t)
            dba_ref[d:d + 1, :] = jnp.sum(dga, axis=0, keepdims=True)
            dbi_ref[d:d + 1, :] = jnp.sum(dgi, axis=0, keepdims=True)
        dcb_ref[...] = jnp.sum(du, axis=0, keepdims=True)
        for j in range(4):
            dcw_ref[j:j + 1, :] = jnp.sum(du * taps[j], axis=0, keepdims=True)
        dup_in = (_shift_rows(du, -2) * cw_ref[0:1, :] + _shift_rows(du, -1) * cw_ref[1:2, :]
                  + du * cw_ref[2:3, :] + _shift_rows(du, 1) * cw_ref[3:4, :])
        duy_ref[0] = dup_in.astype(BF16)

    wshape = jax.ShapeDtypeStruct((2, N_CG, CG, REC_BLOCK), F32)
    vec = lambda rows: jax.ShapeDtypeStruct((rows, D_REC), F32)
    dup_np = _dup_table()
    return pl.pallas_call(
        body, name="rec_bwd",
        out_shape=(jax.ShapeDtypeStruct((2, T, D_REC), BF16),
                   vec(4), vec(1), wshape, vec(2), wshape, vec(2), vec(2)),
        grid=(N_CG,),
        in_specs=[tok(0), tok(N_CG), tok(0), tok(0), pl.BlockSpec((4, T, CG), lambda g: (0, 0, g)), tok(0),
                  per_ch(4), per_ch(1), wspec, per_ch(2), wspec, per_ch(2), per_ch(2),
                  const((REC_BLOCK, CG)), const((CG, REC_BLOCK)), const((CG, CG)),
                  pl.BlockSpec(memory_space=pl.ANY)],
        out_specs=(pl.BlockSpec((2, T, CG), lambda g: (0, 0, g)),
                   per_ch(4), per_ch(1), wspec, per_ch(2), wspec, per_ch(2), per_ch(2)),
        scratch_shapes=[pltpu.VMEM((T, CG), F32)] * 5,
        compiler_params=_params(dimension_semantics=("parallel",)),
    )(uy, uy, hf, hb, am, dyrec, conv_w, conv_b, w_a, b_a, w_i, b_i, lam,
      jnp.asarray(dup_np, BF16), jnp.asarray(dup_np.T.copy()), jnp.asarray(_pair_mask()), after)


TM_MIX = 256


def _mix_specs():
    tok = lambda width, blk=0: pl.BlockSpec((TM_MIX, width), lambda i: (i, blk))
    full = lambda shape: pl.BlockSpec(shape, lambda i: (0, 0))
    return tok, full


def _mix_fwd(x, att, yrec, gg, w_att_o_t, w_rec_o, w_out):
    tok, full = _mix_specs()

    def body(x_ref, att_ref, yr_ref, ga_ref, gr_ref, wao_ref, wro_ref, wo_ref, x1_ref, mixed_ref):
        y_att = _dot_nt(att_ref[...], wao_ref[...])
        y_rec = _dot(yr_ref[...], wro_ref[...])
        mixed = (_sigmoid(ga_ref[...]) * y_att + _sigmoid(gr_ref[...]) * y_rec).astype(BF16)
        mixed_ref[...] = mixed
        x1_ref[...] = x_ref[...] + _dot(mixed, wo_ref[...])

    return pl.pallas_call(
        body, name="mix_fwd",
        out_shape=(jax.ShapeDtypeStruct((T, D), F32), jax.ShapeDtypeStruct((T, D), BF16)),
        grid=(T // TM_MIX,),
        in_specs=[tok(D), tok(D_ATT), tok(D_REC), tok(D, 0), tok(D, 1),
                  full((D, D_ATT)), full((D_REC, D)), full((D, D))],
        out_specs=(tok(D), tok(D)),
        compiler_params=_params(dimension_semantics=("parallel",)),
    )(x, att, yrec, gg, gg, w_att_o_t, w_rec_o, w_out)


def _mix_bwd(dx1, att, yrec, gg, w_att_o_t, w_rec_o, w_out, after):
    tok, full = _mix_specs()

    def body(dx_ref, att_ref, yr_ref, ga_ref, gr_ref, wao_ref, wro_ref, wo_ref, after_ref,
             dgg_ref, dya_ref, dyr_ref, datt_ref, dyrp_ref):
        dmixed = _dot_nt(dx_ref[...].astype(BF16), wo_ref[...])
        y_att = _dot_nt(att_ref[...], wao_ref[...])
        y_rec = _dot(yr_ref[...], wro_ref[...])
        sa = _sigmoid(ga_ref[...])
        sr = _sigmoid(gr_ref[...])
        dgg_ref[0] = (dmixed * y_att * sa * (1.0 - sa)).astype(BF16)
        dgg_ref[1] = (dmixed * y_rec * sr * (1.0 - sr)).astype(BF16)
        dya = (dmixed * sa).astype(BF16)
        dyr = (dmixed * sr).astype(BF16)
        dya_ref[...] = dya
        dyr_ref[...] = dyr
        datt_ref[...] = _dot(dya, wao_ref[...]).astype(BF16)
        dyrp_ref[...] = _dot_nt(dyr, wro_ref[...])

    return pl.pallas_call(
        body, name="mix_bwd",
        out_shape=(jax.ShapeDtypeStruct((2, T, D), BF16),
                   jax.ShapeDtypeStruct((T, D), BF16), jax.ShapeDtypeStruct((T, D), BF16),
                   jax.ShapeDtypeStruct((T, D_ATT), BF16), jax.ShapeDtypeStruct((T, D_REC), F32)),
        grid=(T // TM_MIX,),
        in_specs=[tok(D), tok(D_ATT), tok(D_REC), tok(D, 0), tok(D, 1),
                  full((D, D_ATT)), full((D_REC, D)), full((D, D)), pl.BlockSpec(memory_space=pl.ANY)],
        out_specs=(pl.BlockSpec((2, TM_MIX, D), lambda i: (0, i, 0)),
                   tok(D), tok(D), tok(D_ATT), tok(D_REC)),
        compiler_params=_params(dimension_semantics=("parallel",)),
    )(dx1, att, yrec, gg, gg, w_att_o_t, w_rec_o, w_out, after)


TM_FFN = 256
FF_CHUNK = 1024


def _ffn_loss(x1, target, g2, gf, w_ff1_t, w_ff2):
    n_chunks = D_FF // FF_CHUNK

    def body(x1_ref, tg_ref, g2_ref, gf_ref, w1_hbm, w2_hbm,
             loss_ref, dx1_ref, h2_ref, act_ref, dpre_ref, dx2_ref, dg2_ref, dgf_ref,
             w1, w2, relu_s):
        i = pl.program_id(0)

        @pl.when(i == 0)
        def _():
            pltpu.sync_copy(w1_hbm, w1)
            pltpu.sync_copy(w2_hbm, w2)
            loss_ref[...] = jnp.zeros_like(loss_ref)
            dg2_ref[...] = jnp.zeros_like(dg2_ref)
            dgf_ref[...] = jnp.zeros_like(dgf_ref)

        x1v = x1_ref[...]
        r2 = lax.rsqrt(jnp.mean(x1v * x1v, axis=-1, keepdims=True) + EPS)
        xh2 = x1v * r2
        h2 = (xh2 * g2_ref[...]).astype(BF16)
        h2_ref[...] = h2
        x2 = x1v
        for c in range(n_chunks):
            ff = slice(c * FF_CHUNK, (c + 1) * FF_CHUNK)
            rl = jnp.maximum(_dot_nt(h2, w1[ff, :]), 0.0)
            relu_s[:, ff] = rl
            act = (rl * rl).astype(BF16)
            act_ref[:, ff] = act
            x2 = x2 + _dot(act, w2[ff, :])
        r3 = lax.rsqrt(jnp.mean(x2 * x2, axis=-1, keepdims=True) + EPS)
        xh3 = x2 * r3
        err = xh3 * gf_ref[...] - tg_ref[...]
        loss_ref[...] += 0.5 * jnp.sum(jnp.mean(err * err, axis=-1, keepdims=True))
        dy = err * (1.0 / D)
        dgf_ref[...] += jnp.sum(dy * xh3, axis=0, keepdims=True)
        dx2 = _rms_bwd(dy, xh3, r3, gf_ref[...])
        dx2_16 = dx2.astype(BF16)
        dx2_ref[...] = dx2_16
        dh2 = jnp.zeros((TM_FFN, D), F32)
        for c in range(n_chunks):
            ff = slice(c * FF_CHUNK, (c + 1) * FF_CHUNK)
            dpre = (_dot_nt(dx2_16, w2[ff, :]) * (2.0 * relu_s[:, ff])).astype(BF16)
            dpre_ref[:, ff] = dpre
            dh2 = dh2 + _dot(dpre, w1[ff, :])
        dg2_ref[...] += jnp.sum(dh2 * xh2, axis=0, keepdims=True)
        dx1_ref[...] = dx2 + _rms_bwd(dh2, xh2, r2, g2_ref[...])

    tok = lambda width: pl.BlockSpec((TM_FFN, width), lambda i: (i, 0))
    vec = pl.BlockSpec((1, D), lambda i: (0, 0))
    hbm = pl.BlockSpec(memory_space=pl.ANY)
    return pl.pallas_call(
        body, name="ffn_loss",
        out_shape=(jax.ShapeDtypeStruct((8, 128), F32), jax.ShapeDtypeStruct((T, D), F32),
                   jax.ShapeDtypeStruct((T, D), BF16), jax.ShapeDtypeStruct((T, D_FF), BF16),
                   jax.ShapeDtypeStruct((T, D_FF), BF16), jax.ShapeDtypeStruct((T, D), BF16),
                   jax.ShapeDtypeStruct((1, D), F32), jax.ShapeDtypeStruct((1, D), F32)),
        grid=(T // TM_FFN,),
        in_specs=[tok(D), tok(D), vec, vec, hbm, hbm],
        out_specs=(pl.BlockSpec((8, 128), lambda i: (0, 0)), tok(D), tok(D), tok(D_FF), tok(D_FF), tok(D),
                   vec, vec),
        scratch_shapes=[pltpu.VMEM((D_FF, D), BF16), pltpu.VMEM((D_FF, D), BF16),
                        pltpu.VMEM((TM_FFN, D_FF), F32)],
        compiler_params=_params(dimension_semantics=("arbitrary",)),
    )(x1, target, g2, gf, w_ff1_t, w_ff2)


def _local_step(x, target, p, late_weights, late_weights_ready, reduce_first, reduce_early, reduce_late):
    bias = _rpb_rows(p["rpb"])
    pairs = lambda w: w.reshape(2, N_CG, CG, REC_BLOCK)
    w_a, w_i = pairs(p["w_rg_a"]), pairs(p["w_rg_i"])
    rec_params = (p["conv_w"], p["conv_b"], w_a, p["b_rg_a"], w_i, p["b_rg_i"], p["lru_lambda"])

    qkv, uy, gg, h = _in_proj(x, p["ln1_g"], p["w_in_t"], p["b_in"], p["later_weights_started"])
    hf, hb, yrec, am = _rec_fwd(uy, *rec_params)
    att = _att_fwd(qkv, bias, late_weights(yrec, 0))
    p = {**p, **late_weights_ready(late_weights(att, 1), 0)}
    x1, mixed = _mix_fwd(x, att, yrec, gg, p["w_att_o_t"], p["w_rec_o"], p["w_out"])
    p = {**p, **late_weights_ready(x1, 1)}
    loss8, dx1, h2, act, dpre, dx2, g_ln2, g_lnf = _ffn_loss(
        x1, target, p["ln2_g"], p["lnf_g"], p["w_ff1_t"], p["w_ff2"])

    grads = {"ln2_g": g_ln2, "lnf_g": g_lnf,
             "w_ff1_t": _matmul(dpre, h2, "tn", BF16, "g_w_ff1"),
             "w_ff2": _matmul(act, dx2, "tn", BF16, "g_w_ff2")}
    dgg, dya, dyr, datt, dyrp = _mix_bwd(dx1, att, yrec, gg, p["w_att_o_t"], p["w_rec_o"], p["w_out"],
                                         reduce_first(grads, None))
    duy, g_cw, g_cb, g_wa, g_ba, g_wi, g_bi, g_lam = _rec_bwd(uy, hf, hb, am, dyrp, *rec_params,
                                                              reduce_first(None, dgg))
    blocks = lambda g: g.reshape(2, N_REC_BLOCKS, REC_BLOCK, REC_BLOCK)
    grads.update({
        "w_att_o_t": _matmul(dya, att, "tn", BF16, "g_w_att_o"),
        "conv_w": g_cw, "conv_b": g_cb, "w_rg_a": blocks(g_wa), "b_rg_a": g_ba,
        "w_rg_i": blocks(g_wi), "b_rg_i": g_bi, "lru_lambda": g_lam,
        "w_rec_o": _matmul(yrec, dyr, "tn", BF16, "g_w_rec_o"),
        "w_out": _matmul(mixed, dx1, "tn", BF16, "g_w_out"),
    })
    dqkv, gbias = _att_bwd(qkv, bias, datt, reduce_early(grads))
    dz = (dqkv, duy, dgg)
    g_w_in_t, g_b_in = _grad_w_in(dz, h)
    grads.update(w_in_t=g_w_in_t, b_in=g_b_in)
    grad_x, g_ln1 = _dh_norm1_bwd(dz, p["w_in_t"], x, p["ln1_g"], dx1, reduce_late(grads))
    grads.update(ln1_g=g_ln1, rpb=_rpb_fold(gbias))
    return loss8[0:1, 0:1], grad_x, grads


MESH_ID = pl.DeviceIdType.MESH
ANY = pl.BlockSpec(memory_space=pl.ANY)

CHAN_BLOCK_ROWS = 32
GATE_ROWS = 2 * 2 * N_REC_BLOCKS * REC_BLOCK * REC_BLOCK // (N_DEV * D)
SECTIONS = (("w_in_t", 704, D), ("w_rec_o", 128, D), ("w_out", 128, D), ("w_ff1_t", 512, D),
            ("w_ff2", 512, D), ("chan", CHAN_BLOCK_ROWS, D), ("w_att_o_t", 128, D_ATT),
            ("gates", GATE_ROWS, D))
N_SEC = len(SECTIONS)
N_CHAN_ROWS = 10
CHAN = (("conv_w", 4), ("b_rg_a", 2), ("b_rg_i", 2), ("lru_lambda", 2))


def _position():
    return lax.axis_index("x"), lax.axis_index("y"), lax.axis_index("c")


def _other_chips(x, y):
    return [(1 - x, y), (x, 1 - y), (1 - x, 1 - y)]


PASS_ON_IDS, PAIR_EARLY_ID, PAIR_LATE_ID, PAIR_FIRST_ID, SMALL_PASS_ON_ID = (1, 4), 2, 3, 5, 6


def _pair_handshake(x, y, c):
    barrier = pltpu.get_barrier_semaphore()
    pl.semaphore_signal(barrier, inc=1, device_id=(x, y, 1 - c), device_id_type=MESH_ID)
    pl.semaphore_wait(barrier, 1)


def _block_of(ref, dev, rows):
    return ref.at[pl.ds(pl.multiple_of(dev * rows, 16), rows)]


def _all_gather(shards, name, pieces):
    ns = len(shards)
    parts = [(s, j * (a.shape[0] // n), a.shape[0] // n)
             for s, (a, n) in enumerate(zip(shards, pieces)) for j in range(n)]
    np_ = len(parts)

    def body(*refs):
        x_refs, out_refs = refs[:ns], refs[ns:2 * ns]
        send_sems, recv_sems, local_sems = refs[2 * ns:]
        x, y, c = _position()
        me, sibling = (x, y, c), (x, y, 1 - c)
        x_nbr, y_nbr, diagonal = _other_chips(x, y)
        north = c == 1
        relay_from = (jnp.where(north, x_nbr[0], y_nbr[0]), jnp.where(north, x_nbr[1], y_nbr[1]))
        relay_to = (jnp.where(north, y_nbr[0], x_nbr[0]), jnp.where(north, y_nbr[1], x_nbr[1]))

        def rows(v, px, py, pc):
            s, r0, r = parts[v]
            start = (4 * px + 2 * py + pc) * shards[s].shape[0] + r0
            return out_refs[s].at[pl.ds(pl.multiple_of(start, 16), r)]

        def own(v):
            s, r0, r = parts[v]
            return x_refs[s].at[pl.ds(r0, r)]

        def copy(k, v, block, to, from_shard=False):
            return pltpu.make_async_remote_copy(
                src_ref=own(v) if from_shard else rows(v, *block), dst_ref=rows(v, *block),
                send_sem=send_sems.at[k * np_ + v], recv_sem=recv_sems.at[k * np_ + v],
                device_id=to, device_id_type=MESH_ID)

        sections = range(np_)
        mine = [pltpu.make_async_copy(own(v), rows(v, *me), local_sems.at[v]) for v in sections]
        sent = [copy(k, v, me, to, True) for v in sections
                for k, to in enumerate((sibling, (*x_nbr, c), (*y_nbr, c)))]
        for cp in mine + sent:
            cp.start()
        for s in sections:
            copy(1, s, (*x_nbr, c), me).wait_recv()
            copy(2, s, (*y_nbr, c), me).wait_recv()
            sent += [copy(3, s, (*relay_from, c), (*relay_to, c)),
                     copy(4, s, (*x_nbr, c), sibling), copy(5, s, (*y_nbr, c), sibling)]
            for cp in sent[-3:]:
                cp.start()
        for s in sections:
            copy(3, s, (*diagonal, c), me).wait_recv()
            sent.append(copy(6, s, (*diagonal, c), sibling))
            sent[-1].start()
        for s in sections:
            copy(0, s, sibling, me).wait_recv()
            for k, chip in ((4, x_nbr), (5, y_nbr), (6, diagonal)):
                copy(k, s, (*chip, 1 - c), me).wait_recv()
        for cp in sent:
            cp.wait_send()
        for cp in mine:
            cp.wait()

    return pl.pallas_call(
        body, name=name,
        out_shape=tuple(jax.ShapeDtypeStruct((N_DEV * s.shape[0], s.shape[1]), s.dtype) for s in shards),
        in_specs=[ANY] * ns,
        out_specs=(ANY,) * ns,
        scratch_shapes=[pltpu.SemaphoreType.DMA((7 * np_,)), pltpu.SemaphoreType.DMA((7 * np_,)),
                        pltpu.SemaphoreType.DMA((np_,))],
    )(*shards)


HBM = pl.BlockSpec(memory_space=pltpu.HBM)
SEM = pl.BlockSpec(memory_space=pltpu.SEMAPHORE)
EFFECT = pltpu.SideEffectType.DATAFLOW_SIDE_EFFECTING


def _in_hbm(a):
    return pltpu.with_memory_space_constraint(a, pltpu.HBM)


def _first_hop_copies(rows, which, x_refs, zones, send_sems, recv_sems):
    ns = len(rows)
    x, y, c = _position()
    targets = [(x, y, 1 - c)] + [(cx, cy, c) for cx, cy in _other_chips(x, y)]
    return [pltpu.make_async_remote_copy(
        src_ref=x_refs[i], dst_ref=_block_of(zones[i], 4 * x + 2 * y + c, rows[s]),
        send_sem=send_sems.at[k * ns + s], recv_sem=recv_sems.at[k * ns + s],
        device_id=to, device_id_type=MESH_ID)
        for k, to in enumerate(targets) for i, s in enumerate(which)]


def _after_all(arrays, name):
    def body(*refs):
        refs[-1][...] = jnp.zeros_like(refs[-1])

    return pl.pallas_call(
        body, name=name,
        out_shape=jax.ShapeDtypeStruct((8, LANES), F32),
        in_specs=[pl.BlockSpec(memory_space=pl.ANY)] * len(arrays),
        out_specs=pl.BlockSpec(memory_space=pltpu.VMEM),
    )(*arrays)


def _own_blocks_placed(shards, after, name):
    ns = len(shards)
    x, y, c = _position()
    me = jnp.reshape(4 * x + 2 * y + c, (1,)).astype(jnp.int32)
    tokens = [] if after is None else [after]

    def body(me_ref, *refs):
        for s in range(ns):
            refs[ns + len(tokens) + s][...] = refs[s][...]

    return pl.pallas_call(
        body, name=name,
        out_shape=tuple(jax.ShapeDtypeStruct((N_DEV * s.shape[0], s.shape[1]), s.dtype) for s in shards),
        grid_spec=pltpu.PrefetchScalarGridSpec(
            num_scalar_prefetch=1, grid=(1,),
            in_specs=[pl.BlockSpec(s.shape, lambda i, me: (0, 0)) for s in shards] + [ANY] * len(tokens),
            out_specs=tuple(pl.BlockSpec(s.shape, lambda i, me: (me[0], 0)) for s in shards)),
        compiler_params=_params(dimension_semantics=("arbitrary",)),
    )(me, *shards, *tokens)


def _gather_start(shards, after, name):
    ns = len(shards)
    zones = _own_blocks_placed(shards, after, name + "_own_blocks")

    def body(*refs):
        for cp in _first_hop_copies([s.shape[0] for s in shards], range(ns), refs[:ns], refs[ns:2 * ns],
                                    refs[2 * ns], refs[2 * ns + 1]):
            cp.start()
        refs[-1][...] = jnp.zeros_like(refs[-1])

    out = pl.pallas_call(
        body, name=name,
        out_shape=(pltpu.SemaphoreType.DMA((4 * ns,)), pltpu.SemaphoreType.DMA((4 * ns,)),
                   *[pltpu.HBM(a.shape, a.dtype) for a in (*shards, *zones)],
                   jax.ShapeDtypeStruct((8, LANES), F32)),
        in_specs=[HBM] * (2 * ns),
        out_specs=(SEM, SEM, *[HBM] * (2 * ns), pl.BlockSpec(memory_space=pltpu.VMEM)),
        input_output_aliases={i: 2 + i for i in range(2 * ns)},
        compiler_params=pltpu.CompilerParams(has_side_effects=EFFECT),
    )(*[_in_hbm(a) for a in shards], *[_in_hbm(a) for a in zones])
    return out[0], out[1], out[2:2 + ns], out[2 + ns:2 + 2 * ns], out[-1]


def _gather_wait(send_sems, recv_sems, rows, which, shards, zones, after, name):
    ns = len(shards)

    def body(*refs):
        for cp in _first_hop_copies(rows, which, refs[:ns], refs[ns:2 * ns], refs[2 * ns], refs[2 * ns + 1]):
            cp.wait_send()
            cp.wait_recv()

    out = pl.pallas_call(
        body, name=name,
        out_shape=tuple(pltpu.HBM(a.shape, a.dtype) for a in (*shards, *zones)),
        in_specs=[HBM] * (2 * ns) + [SEM, SEM, ANY],
        out_specs=(HBM,) * (2 * ns),
        input_output_aliases={i: i for i in range(2 * ns)},
        compiler_params=pltpu.CompilerParams(has_side_effects=EFFECT),
    )(*shards, *zones, send_sems, recv_sems, after)
    return out[:ns], out[ns:]


def _pass_on_copies(rows, in_refs, out_refs, send_sems, recv_sems):
    ns = len(rows)
    x, y, c = _position()
    return [pltpu.make_async_remote_copy(
        src_ref=_block_of(in_refs[s], 4 * cx + 2 * cy + c, rows[s]),
        dst_ref=_block_of(out_refs[s], 4 * cx + 2 * cy + c, rows[s]),
        send_sem=send_sems.at[j * ns + s], recv_sem=recv_sems.at[j * ns + s],
        device_id=(x, y, 1 - c), device_id_type=MESH_ID)
        for j, (cx, cy) in enumerate(_other_chips(x, y)) for s in range(ns)]


def _pass_on_start(rows, zones, barrier_id, name):
    ns = len(zones)

    def body(*refs):
        _pair_handshake(*_position())
        for cp in _pass_on_copies(rows, refs[:ns], refs[:ns], refs[ns], refs[ns + 1]):
            cp.start()
        refs[-1][...] = jnp.zeros_like(refs[-1])

    out = pl.pallas_call(
        body, name=name,
        out_shape=(pltpu.SemaphoreType.DMA((3 * ns,)), pltpu.SemaphoreType.DMA((3 * ns,)),
                   *[pltpu.HBM(z.shape, z.dtype) for z in zones], jax.ShapeDtypeStruct((8, LANES), F32)),
        in_specs=[HBM] * ns,
        out_specs=(SEM, SEM, *[HBM] * ns, pl.BlockSpec(memory_space=pltpu.VMEM)),
        input_output_aliases={i: 2 + i for i in range(ns)},
        compiler_params=pltpu.CompilerParams(has_side_effects=EFFECT, collective_id=barrier_id),
    )(*[_in_hbm(z) for z in zones])
    return out[0], out[1], out[2:2 + ns], out[-1]


def _pass_on_wait(rows, send_sems, recv_sems, zones, after, name):
    ns = len(zones)

    def body(*refs):
        for cp in _pass_on_copies(rows, refs[:ns], refs[:ns], refs[ns], refs[ns + 1]):
            cp.wait_send()
            cp.wait_recv()

    return pl.pallas_call(
        body, name=name,
        out_shape=tuple(pltpu.HBM(z.shape, z.dtype) for z in zones),
        in_specs=[HBM] * ns + [SEM, SEM, ANY],
        out_specs=(HBM,) * ns,
        input_output_aliases={i: i for i in range(ns)},
        compiler_params=pltpu.CompilerParams(has_side_effects=EFFECT),
    )(*zones, send_sems, recv_sems, after)


def _gather_pass_on(rows, zones, barrier_id, name):
    ns = len(zones)

    def body(*refs):
        _pair_handshake(*_position())
        copies = _pass_on_copies(rows, refs[:ns], refs[ns:2 * ns], *refs[2 * ns:])
        for cp in copies:
            cp.start()
        for cp in copies:
            cp.wait_recv()
        for cp in copies:
            cp.wait_send()

    return pl.pallas_call(
        body, name=name,
        out_shape=tuple(jax.ShapeDtypeStruct(z.shape, z.dtype) for z in zones),
        in_specs=[ANY] * ns, out_specs=(ANY,) * ns,
        input_output_aliases={i: i for i in range(ns)},
        scratch_shapes=[pltpu.SemaphoreType.DMA((3 * ns,)), pltpu.SemaphoreType.DMA((3 * ns,))],
        compiler_params=pltpu.CompilerParams(collective_id=barrier_id),
    )(*zones)


def _pair_copies(sections, g_refs, land, send_sems, recv_sems):
    ns = len(sections)
    x, y, c = _position()
    return [pltpu.make_async_remote_copy(
        src_ref=_block_of(g_refs[s], 2 * k + 1 - c, rows), dst_ref=land[s].at[k],
        send_sem=send_sems.at[k * ns + s], recv_sem=recv_sems.at[k * ns + s],
        device_id=(x, y, 1 - c), device_id_type=MESH_ID)
        for k in range(N_CHIPS) for s, (_, rows, _) in enumerate(sections)]


def _pair_exchange_start(sections, grads, barrier_id, name):
    ns = len(sections)

    def body(*refs):
        _pair_handshake(*_position())
        for cp in _pair_copies(sections, refs[:ns], refs[ns:2 * ns], refs[2 * ns], refs[2 * ns + 1]):
            cp.start()
        refs[-1][...] = jnp.zeros_like(refs[-1])

    zones = [lax.empty((N_CHIPS, rows, cols), BF16) for _, rows, cols in sections]
    n = N_CHIPS * ns
    out = pl.pallas_call(
        body, name=name,
        out_shape=(pltpu.SemaphoreType.DMA((n,)), pltpu.SemaphoreType.DMA((n,)),
                   *[pltpu.HBM(a.shape, a.dtype) for a in (*grads, *zones)],
                   jax.ShapeDtypeStruct((8, LANES), F32)),
        in_specs=[HBM] * (2 * ns),
        out_specs=(SEM, SEM, *[HBM] * (2 * ns), pl.BlockSpec(memory_space=pltpu.VMEM)),
        input_output_aliases={i: 2 + i for i in range(2 * ns)},
        compiler_params=pltpu.CompilerParams(has_side_effects=EFFECT, collective_id=barrier_id),
    )(*[_in_hbm(a) for a in grads], *[_in_hbm(a) for a in zones])
    return out[0], out[1], out[2:2 + ns], out[2 + ns:2 + 2 * ns], out[-1]


def _pair_exchange_wait(sections, send_sems, recv_sems, grads, zones, after, name):
    ns = len(sections)

    def body(*refs):
        for cp in _pair_copies(sections, refs[:ns], refs[ns:2 * ns], refs[2 * ns], refs[2 * ns + 1]):
            cp.wait_send()
            cp.wait_recv()

    out = pl.pallas_call(
        body, name=name,
        out_shape=tuple(pltpu.HBM(a.shape, a.dtype) for a in (*grads, *zones)),
        in_specs=[HBM] * (2 * ns) + [SEM, SEM, ANY],
        out_specs=(HBM,) * (2 * ns),
        input_output_aliases={i: i for i in range(2 * ns)},
        compiler_params=pltpu.CompilerParams(has_side_effects=EFFECT),
    )(*grads, *zones, send_sems, recv_sems, after)
    return out[:ns], out[ns:]


def _pair_add(sections, grads, got, core, name):
    ns = len(sections)

    def body(core_ref, *refs):
        g_refs, got_refs, p_refs = refs[:ns], refs[ns:2 * ns], refs[2 * ns:]
        for s in range(ns):
            p_refs[s][0] = (g_refs[s][...].astype(F32) + got_refs[s][0].astype(F32)).astype(BF16)

    slot = [pl.BlockSpec((1, rows, cols), lambda k, c: (k, 0, 0)) for _, rows, cols in sections]
    return pl.pallas_call(
        body, name=name,
        out_shape=tuple(jax.ShapeDtypeStruct((N_CHIPS, rows, cols), BF16) for _, rows, cols in sections),
        grid_spec=pltpu.PrefetchScalarGridSpec(
            num_scalar_prefetch=1, grid=(N_CHIPS,),
            in_specs=[pl.BlockSpec((rows, cols), lambda k, c: (2 * k + c[0], 0)) for _, rows, cols in sections]
            + slot,
            out_specs=tuple(slot)),
        compiler_params=_params(dimension_semantics=("parallel",)),
    )(core, *grads, *got)


def _chip_copies(sections, p_refs, land, send_sems, recv_sems):
    ns = len(sections)
    x, y, c = _position()
    return [pltpu.make_async_remote_copy(
        src_ref=p_refs[s].at[2 * cx + cy], dst_ref=land[s].at[j],
        send_sem=send_sems.at[j * ns + s], recv_sem=recv_sems.at[j * ns + s],
        device_id=(cx, cy, c), device_id_type=MESH_ID)
        for j, (cx, cy) in enumerate(_other_chips(x, y)) for s in range(ns)]


def _chip_exchange(sections, parts, name):
    ns = len(sections)

    def body(*refs):
        copies = _chip_copies(sections, refs[:ns], refs[ns:2 * ns], *refs[2 * ns:])
        for cp in copies:
            cp.start()
        for cp in copies:
            cp.wait_recv()
        for cp in copies:
            cp.wait_send()

    n = 3 * ns
    return pl.pallas_call(
        body, name=name,
        out_shape=tuple(jax.ShapeDtypeStruct((3, rows, cols), BF16) for _, rows, cols in sections),
        in_specs=[ANY] * ns, out_specs=(ANY,) * ns,
        scratch_shapes=[pltpu.SemaphoreType.DMA((n,)), pltpu.SemaphoreType.DMA((n,))],
    )(*parts)


def _chip_exchange_start(sections, parts, name):
    ns = len(sections)

    def body(*refs):
        p_refs, land = refs[:ns], refs[ns:2 * ns]
        send_sems, recv_sems = refs[2 * ns], refs[2 * ns + 1]
        token = refs[-1]
        for cp in _chip_copies(sections, p_refs, land, send_sems, recv_sems):
            cp.start()
        token[...] = jnp.zeros_like(token)

    zones = [lax.empty((3, rows, cols), BF16) for _, rows, cols in sections]
    out = pl.pallas_call(
        body, name=name,
        out_shape=(pltpu.SemaphoreType.DMA((3 * ns,)), pltpu.SemaphoreType.DMA((3 * ns,)),
                   *[pltpu.HBM(a.shape, a.dtype) for a in parts], *[pltpu.HBM(a.shape, a.dtype) for a in zones],
                   jax.ShapeDtypeStruct((8, LANES), F32)),
        in_specs=[HBM] * (2 * ns),
        out_specs=(SEM, SEM, *[HBM] * (2 * ns), pl.BlockSpec(memory_space=pltpu.VMEM)),
        input_output_aliases={i: 2 + i for i in range(2 * ns)},
        compiler_params=pltpu.CompilerParams(has_side_effects=EFFECT),
    )(*[_in_hbm(a) for a in parts], *[_in_hbm(a) for a in zones])
    return out[0], out[1], out[2:2 + ns], out[2 + ns:2 + 2 * ns], out[-1]


def _chip_exchange_wait(sections, send_sems, recv_sems, parts, zones, after, name):
    ns = len(sections)

    def body(*refs):
        p_refs, land = refs[:ns], refs[ns:2 * ns]
        for cp in _chip_copies(sections, p_refs, land, refs[2 * ns], refs[2 * ns + 1]):
            cp.wait_send()
            cp.wait_recv()

    out = pl.pallas_call(
        body, name=name,
        out_shape=tuple(pltpu.HBM(a.shape, a.dtype) for a in (*parts, *zones)),
        in_specs=[HBM] * (2 * ns) + [SEM, SEM, ANY],
        out_specs=(HBM,) * (2 * ns),
        input_output_aliases={i: i for i in range(2 * ns)},
        compiler_params=pltpu.CompilerParams(has_side_effects=EFFECT),
    )(*parts, *zones, send_sems, recv_sems, after)
    return out[:ns], out[ns:]


def _grad_finish(sections, parts, far, chip, name):
    ns = len(sections)

    def body(chip_ref, *refs):
        p_refs, b_refs, g_refs = refs[:ns], refs[ns:2 * ns], refs[2 * ns:]
        for s in range(ns):
            g = p_refs[s][0].astype(F32)
            for j in range(3):
                g = g + b_refs[s][j].astype(F32)
            g_refs[s][...] = g

    half = [(rows // 2, cols) for _, rows, cols in sections]
    return pl.pallas_call(
        body, name=name,
        out_shape=tuple(jax.ShapeDtypeStruct((rows, cols), F32) for _, rows, cols in sections),
        grid_spec=pltpu.PrefetchScalarGridSpec(
            num_scalar_prefetch=1, grid=(2,),
            in_specs=[pl.BlockSpec((1, r, c), lambda i, chip: (chip[0], i, 0)) for r, c in half]
            + [pl.BlockSpec((3, r, c), lambda i, chip: (0, i, 0)) for r, c in half],
            out_specs=tuple(pl.BlockSpec((r, c), lambda i, chip: (i, 0)) for r, c in half)),
        compiler_params=_params(dimension_semantics=("parallel",)),
    )(chip, *parts, *far)


def _sum_devices(parts, rows, name):
    cols = parts.shape[1]
    tr = rows // 2

    def body(*refs):
        s = refs[0][...].astype(F32)
        for d in range(1, N_DEV):
            s = s + refs[d][...].astype(F32)
        refs[N_DEV][...] = s

    return pl.pallas_call(
        body, name=name,
        out_shape=jax.ShapeDtypeStruct((rows, cols), F32),
        grid=(2,),
        in_specs=[pl.BlockSpec((tr, cols), lambda i, d=d: (2 * d + i, 0)) for d in range(N_DEV)],
        out_specs=pl.BlockSpec((tr, cols), lambda i: (i, 0)),
        compiler_params=_params(dimension_semantics=("parallel",)),
    )(*([parts] * N_DEV))


def _adamw_step(w_ref, g_ref, m_ref, v_ref, d_ref, nm_ref, nv_ref):
    c1 = 1.0 / (1.0 - ADAM_B1 ** ADAM_STEP)
    c2 = 1.0 / (1.0 - ADAM_B2 ** ADAM_STEP)
    gv = g_ref[...]
    nm = ADAM_B1 * m_ref[...] + (1.0 - ADAM_B1) * gv
    nv = ADAM_B2 * v_ref[...] + (1.0 - ADAM_B2) * (gv * gv)
    nm_ref[...] = nm
    nv_ref[...] = nv
    d_ref[...] = (-ADAM_LR) * ((nm * c1) / (jnp.sqrt(nv * c2) + ADAM_EPS) + ADAM_WD * w_ref[...])


def _adamw_small(params, name):
    n = len(params)

    def body(*refs):
        for k in range(n):
            _adamw_step(*refs[4 * k:4 * k + 4], *refs[4 * n + 3 * k:4 * n + 3 * k + 3])

    out = pl.pallas_call(
        body, name=name,
        out_shape=tuple(jax.ShapeDtypeStruct(p[0].shape, F32) for p in params for _ in range(3)),
    )(*[a for p in params for a in p])
    return [out[3 * k:3 * k + 3] for k in range(n)]


def _adamw(w, g, m, v, name, after=None):
    rows, cols = w.shape
    tr = rows
    while tr * cols * 4 > (1 << 20) and tr % 16 == 0:
        tr //= 2
    tokens = [] if after is None else [after]

    def body(*refs):
        _adamw_step(*refs[:4], *refs[4 + len(tokens):])

    spec = pl.BlockSpec((tr, cols), lambda i: (i, 0))
    shape = jax.ShapeDtypeStruct((rows, cols), F32)
    return pl.pallas_call(
        body, name=name,
        out_shape=(shape, shape, shape),
        grid=(rows // tr,),
        in_specs=[spec] * 4 + [ANY] * len(tokens), out_specs=(spec,) * 3,
        compiler_params=_params(dimension_semantics=("parallel",)),
    )(w, g, m, v, *tokens)


NAMES = ("ln1_g", "w_in", "b_in", "rpb", "w_att_o", "conv_w", "conv_b", "w_rg_a", "b_rg_a", "w_rg_i",
         "b_rg_i", "lru_lambda", "w_rec_o", "w_out", "ln2_g", "w_ff1", "w_ff2", "lnf_g")
TRANSPOSED = {"w_in": "w_in_t", "w_att_o": "w_att_o_t", "w_ff1": "w_ff1_t"}
ROW_SHARDED = ("w_rec_o", "w_out", "w_ff2")
REPLICATED = (("ln1_g", (1, D)), ("b_in", (1, D_IN)), ("rpb", (N_HEADS * N_RPB_R, N_RPB_C)),
              ("conv_b", (1, D_REC)), ("w_rg_a", (2 * N_REC_BLOCKS * REC_BLOCK, REC_BLOCK)),
              ("w_rg_i", (2 * N_REC_BLOCKS * REC_BLOCK, REC_BLOCK)), ("ln2_g", (1, D)), ("lnf_g", (1, D)))
GATE_BLOCKS = ("w_rg_a", "w_rg_i")
SMALL_ROWS = 112


def _chan_bits(vectors):
    chan = jnp.concatenate(vectors, axis=0)
    bits = lax.bitcast_convert_type(chan, BF16).reshape(-1)
    return jnp.pad(bits, (0, CHAN_BLOCK_ROWS * D - bits.shape[0])).reshape(CHAN_BLOCK_ROWS, D)


def _chan_from_bits(gathered):
    bits = gathered.reshape(N_DEV, CHAN_BLOCK_ROWS * D)[:, :2 * N_CHAN_ROWS * LANES]
    chan = lax.bitcast_convert_type(bits.reshape(N_DEV, N_CHAN_ROWS, LANES, 2), F32)
    return chan.transpose(1, 0, 2).reshape(N_CHAN_ROWS, D)


def kernel(x, ln1_g, w_in, b_in, rpb, w_att_o, conv_w, conv_b, w_rg_a, b_rg_a, w_rg_i, b_rg_i, lru_lambda, w_rec_o, w_out, ln2_g, w_ff1, w_ff2, lnf_g, loss_target, m_ln1_g, m_w_in, m_b_in, m_rpb, m_w_att_o, m_conv_w, m_conv_b, m_w_rg_a, m_b_rg_a, m_w_rg_i, m_b_rg_i, m_lru_lambda, m_w_rec_o, m_w_out, m_ln2_g, m_w_ff1, m_w_ff2, m_lnf_g, v_ln1_g, v_w_in, v_b_in, v_rpb, v_w_att_o, v_conv_w, v_conv_b, v_w_rg_a, v_b_rg_a, v_w_rg_i, v_b_rg_i, v_lru_lambda, v_w_rec_o, v_w_out, v_ln2_g, v_w_ff1, v_w_ff2, v_lnf_g):
    w = dict(zip(NAMES, (ln1_g, w_in, b_in, rpb, w_att_o, conv_w, conv_b, w_rg_a, b_rg_a, w_rg_i,
                         b_rg_i, lru_lambda, w_rec_o, w_out, ln2_g, w_ff1, w_ff2, lnf_g)))
    m = dict(zip(NAMES, (m_ln1_g, m_w_in, m_b_in, m_rpb, m_w_att_o, m_conv_w, m_conv_b, m_w_rg_a,
                         m_b_rg_a, m_w_rg_i, m_b_rg_i, m_lru_lambda, m_w_rec_o, m_w_out, m_ln2_g,
                         m_w_ff1, m_w_ff2, m_lnf_g)))
    v = dict(zip(NAMES, (v_ln1_g, v_w_in, v_b_in, v_rpb, v_w_att_o, v_conv_w, v_conv_b, v_w_rg_a,
                         v_b_rg_a, v_w_rg_i, v_b_rg_i, v_lru_lambda, v_w_rec_o, v_w_out, v_ln2_g,
                         v_w_ff1, v_w_ff2, v_lnf_g)))
    xi, yi, ci = _position()

    shard = {t: w[n][0].T.astype(BF16) for n, t in TRANSPOSED.items()}
    shard.update({n: w[n][0].astype(BF16) for n in ROW_SHARDED})
    shard["chan"] = _chan_bits([w[n][0] for n, _ in CHAN])
    first, later = ("w_in_t", "chan"), ("w_rec_o", "w_out", "w_att_o_t", "w_ff1_t", "w_ff2")
    p = dict(zip(first, _all_gather([shard[n] for n in first], "weight_all_gather", (11, 1))))
    send_sems, recv_sems, sent, zones, token = _gather_start([shard[n] for n in later], p["w_in_t"],
                                                             "weight_gather_start")

    stages = (("w_rec_o", "w_out", "w_att_o_t"), ("w_ff1_t", "w_ff2"))
    passing = {}

    def late_weights(after, stage):
        which = [later.index(n) for n in stages[stage]]
        _, arrived = _gather_wait(
            send_sems, recv_sems, [shard[n].shape[0] for n in later], which, [sent[i] for i in which],
            [zones[i] for i in which], after, "weight_gather_wait_%d" % stage)
        passing[stage] = _pass_on_start(
            [shard[n].shape[0] for n in stages[stage]], arrived,
            PASS_ON_IDS[stage], "weight_pass_on_start_%d" % stage)
        return passing[stage][-1]

    def late_weights_ready(after, stage):
        pass_send_sems, pass_recv_sems, pass_zones, _ = passing[stage]
        return dict(zip(stages[stage], _pass_on_wait(
            [shard[n].shape[0] for n in stages[stage]], pass_send_sems, pass_recv_sems, pass_zones, after,
            "weight_pass_on_wait_%d" % stage)))

    chan = _chan_from_bits(p.pop("chan"))
    r0 = 0
    for n, rows in CHAN:
        p[n] = chan[r0:r0 + rows]
        r0 += rows
    p.update(ln1_g=w["ln1_g"], b_in=w["b_in"], later_weights_started=token, rpb=w["rpb"][0], conv_b=w["conv_b"],
             w_rg_a=w["w_rg_a"][0], w_rg_i=w["w_rg_i"][0], ln2_g=w["ln2_g"],
             lnf_g=w["lnf_g"].reshape(1, D))

    core = jnp.reshape(ci, (1,)).astype(jnp.int32)
    chip = jnp.reshape(2 * xi + yi, (1,)).astype(jnp.int32)
    first_sections = tuple(s for s in SECTIONS if s[0] in ("w_ff1_t", "w_ff2"))
    late_sections = SECTIONS[:1]
    early_sections = tuple(s for s in SECTIONS[1:] if s not in first_sections)
    in_flight = {}

    def pair_sum_and_send(group, sections, after):
        send_sems, recv_sems, sect, zones, _ = in_flight["pair_" + group]
        sect, got = _pair_exchange_wait(sections, send_sems, recv_sems, sect, zones, after,
                                        "grad_pair_exchange_wait_" + group)
        parts = _pair_add(sections, sect, got, core, "grad_pair_add_" + group)
        in_flight[group] = _chip_exchange_start(sections, parts, "grad_chip_exchange_start_" + group)
        return in_flight[group][-1]

    def pair_exchange_at_once(group, sections, grads, barrier_id):
        in_flight["pair_" + group] = _pair_exchange_start(
            sections, [grads[n] for n, _, _ in sections], barrier_id, "grad_pair_exchange_start_" + group)
        return pair_sum_and_send(group, sections, in_flight["pair_" + group][-1])

    def reduce_first(grads, after):
        if grads is None:
            return pair_sum_and_send("first", first_sections, after)
        in_flight["pair_first"] = _pair_exchange_start(
            first_sections, [grads[n] for n, _, _ in first_sections], PAIR_FIRST_ID,
            "grad_pair_exchange_start_first")
        return in_flight["pair_first"][-1]

    def reduce_early(grads):
        chan_g = jnp.concatenate([grads[n] for n, _ in CHAN], axis=0)
        chan_g = chan_g.reshape(N_CHAN_ROWS, N_DEV, LANES).transpose(1, 0, 2).astype(BF16)
        chan_g = jnp.pad(chan_g.reshape(N_DEV, -1), ((0, 0), (0, CHAN_BLOCK_ROWS * D - N_CHAN_ROWS * LANES)))
        grads["chan"] = chan_g.reshape(N_DEV * CHAN_BLOCK_ROWS, D)
        grads["gates"] = jnp.concatenate([grads[n].reshape(-1, D) for n in GATE_BLOCKS], axis=0).astype(BF16)
        return pair_exchange_at_once("early", early_sections, grads, PAIR_EARLY_ID)

    def finish(group, sections, after, name):
        send_sems, recv_sems, parts, zones, _ = in_flight[group]
        parts, far = _chip_exchange_wait(sections, send_sems, recv_sems, parts, zones, after,
                                         "grad_chip_exchange_wait_" + name)
        return dict(zip((n for n, _, _ in sections),
                        _grad_finish(sections, parts, far, chip, "grad_finish_" + name)))

    summed = {}

    def reduce_late(grads):
        in_flight["pair_late"] = _pair_exchange_start(
            late_sections, [grads[n] for n, _, _ in late_sections], PAIR_LATE_ID,
            "grad_pair_exchange_start_late")
        summed.update(finish("first", first_sections, in_flight["pair_late"][-1], "first"))
        summed.update(finish("early", early_sections, summed["w_ff2"], "early"))
        return pair_sum_and_send("late", late_sections, summed["gates"])

    loss_part, grad_x, grads = _local_step(x[0], loss_target[0], p, late_weights, late_weights_ready,
                                           reduce_first, reduce_early, reduce_late)

    flat = jnp.concatenate([grads[n].reshape(-1) for n, _ in REPLICATED if n not in GATE_BLOCKS]
                           + [loss_part.reshape(-1)])
    n_small = flat.shape[0]
    flat = jnp.pad(flat, (0, SMALL_ROWS * LANES - n_small)).reshape(SMALL_ROWS, LANES)
    *small_gather, small_started = _gather_start([flat, summed["gates"]], None, "small_grad_gather_start")

    g, delta, new_m, new_v = {}, {}, {}, {}

    def update(n, g2, shape2, after=None):
        d2, m2, v2 = _adamw(w[n].reshape(shape2), g2, m[n].reshape(shape2), v[n].reshape(shape2),
                            "adamw_" + n, after)
        g[n], delta[n], new_m[n], new_v[n] = (a.reshape(w[n].shape) for a in (g2, d2, m2, v2))

    for n in ROW_SHARDED:
        update(n, summed[n], summed[n].shape, small_started)
    for n, t in TRANSPOSED.items():
        if t in summed:
            update(n, summed[t].T, summed[t].shape[::-1], small_started)

    small_rows = [SMALL_ROWS, GATE_ROWS]
    _, small_zones = _gather_wait(
        *small_gather[:2], small_rows, range(2), *small_gather[2:],
        _after_all(list(delta.values()), "sharded_updates_done"), "small_grad_gather_wait")
    small_parts, gate_sum = _gather_pass_on(small_rows, small_zones, SMALL_PASS_ON_ID,
                                            "small_grad_gather_pass_on")
    small = _sum_devices(small_parts, SMALL_ROWS, "small_grad_sum").reshape(-1)
    loss = small[n_small - 1]

    small_params = []
    o = 0
    for n, shape2 in REPLICATED:
        if n in GATE_BLOCKS:
            k, rows = GATE_BLOCKS.index(n), gate_sum.shape[0] // len(GATE_BLOCKS)
            update(n, gate_sum[k * rows:(k + 1) * rows].reshape(shape2), shape2)
        else:
            size = shape2[0] * shape2[1]
            small_params.append((n, small[o:o + size].reshape(shape2), shape2))
            o += size
    chan_back = summed["chan"].reshape(-1)[:N_CHAN_ROWS * LANES].reshape(N_CHAN_ROWS, LANES)
    r0 = 0
    for n, rows in CHAN:
        small_params.append((n, chan_back[r0:r0 + rows], (rows, LANES)))
        r0 += rows
    results = _adamw_small([(w[n].reshape(s2), g2, m[n].reshape(s2), v[n].reshape(s2))
                            for n, g2, s2 in small_params], "adamw_vectors")
    for (n, g2, _), (d2, m2, v2) in zip(small_params, results):
        g[n], delta[n], new_m[n], new_v[n] = (a.reshape(w[n].shape) for a in (g2, d2, m2, v2))

    summed = finish("late", late_sections, _after_all(list(delta.values()), "updates_done"), "late")
    g_t = summed["w_in_t"]
    results = _adamw(w["w_in"][0].T, g_t, m["w_in"][0].T, v["w_in"][0].T, "adamw_w_in")
    g["w_in"], delta["w_in"], new_m["w_in"], new_v["w_in"] = (a.T[None] for a in (g_t, *results))

    return (loss, grad_x[None], *[g[n] for n in NAMES], *[delta[n] for n in NAMES],
            *[new_m[n] for n in NAMES], *[new_v[n] for n in NAMES])
```

```python
import math

import numpy as np
import jax
import jax.numpy as jnp
from jax import lax
from jax.experimental import pallas as pl
from jax.experimental.pallas import tpu as pltpu

F32 = jnp.float32
BF16 = jnp.bfloat16

T = 2048
D = 1024
D_ATT = 512
D_REC = 1024
D_FF = 4096
D_IN = 5632
N_HEADS = 8
DH = 64
GRID_W = 64
ROWS = T // GRID_W
WIN_H = 8
WIN_W = 16
KWIN = WIN_H * GRID_W
N_RPB_R = 2 * WIN_H - 1
N_RPB_C = 2 * WIN_W - 1
N_REC_BLOCKS = 16
REC_BLOCK = 64
CG = 128
N_CG = D_REC // CG
LRU_C = 8.0
EPS = 1e-6
N_DEV = 8
N_CHIPS = 4
LANES = 128

ADAM_LR = 0.001
ADAM_B1 = 0.9
ADAM_B2 = 0.999
ADAM_EPS = 1e-08
ADAM_WD = 0.01
ADAM_STEP = 10

MESH_AXES = ("x", "y", "c")
VMEM_LIMIT = 56 * 1024 * 1024

TILE = 512
DZ_ARRAYS = ((0, 3, 1), (3, 4, 2), (7, 4, 2))
N_DZ_TILES = D_IN // TILE


def _params(**kw):
    return pltpu.CompilerParams(vmem_limit_bytes=VMEM_LIMIT, **kw)


HG = 4
HQ = HG * GRID_W
HC = HG * DH


def _att_tables():
    rq = np.arange(GRID_W)
    kc = np.arange(KWIN) % GRID_W
    win_start = np.clip(rq - WIN_W // 2, 0, GRID_W - WIN_W)
    valid = (kc[None, :] >= win_start[:, None]) & (kc[None, :] < win_start[:, None] + WIN_W)
    same_head = (np.arange(HQ)[:, None] // GRID_W) == (np.arange(HC)[None, :] // DH)
    return valid.astype(np.float32), same_head.astype(np.float32)


def _pair_mask():
    half = np.arange(2 * DH) // DH
    return (half[:, None] == half[None, :]).astype(np.float32)


def _dup_table():
    return np.concatenate([np.eye(REC_BLOCK, dtype=np.float32)] * 2, axis=1)


def _sigmoid(x):
    return 0.5 * jnp.tanh(0.5 * x) + 0.5


def _softplus(x):
    return jnp.maximum(x, 0.0) + jnp.log(1.0 + jnp.exp(-jnp.abs(x)))


def _one_minus_square(log_a, a):
    x = 2.0 * log_a
    series = -x * (1.0 + x * (0.5 + x * (1.0 / 6.0)))
    return jnp.where(x > -0.02, series, 1.0 - a * a)


_GELU_C = math.sqrt(2.0 / math.pi)


def _gelu_and_grad(x):
    x2 = x * x
    inner = _GELU_C * (x + 0.044715 * x * x2)
    t = jnp.tanh(inner)
    g = 0.5 * x * (1.0 + t)
    dg = 0.5 * (1.0 + t) + 0.5 * x * (1.0 - t * t) * _GELU_C * (1.0 + 3.0 * 0.044715 * x2)
    return g, dg


def _dot(a, b):
    return jnp.dot(a, b, preferred_element_type=F32)


def _dot_nt(a, b):
    return lax.dot_general(a, b, (((1,), (1,)), ((), ())), preferred_element_type=F32)


def _dot_tn(a, b):
    return lax.dot_general(a, b, (((0,), (0,)), ((), ())), preferred_element_type=F32)


def _dot_exact(a, b):
    return jnp.dot(a, b, precision=lax.Precision.HIGHEST, preferred_element_type=F32)


def _shift_rows(x, s):
    n = x.shape[0]
    rows = lax.broadcasted_iota(jnp.int32, x.shape, 0)
    y = pltpu.roll(x, s % n, 0)
    if s > 0:
        return jnp.where(rows >= s, y, 0.0)
    return jnp.where(rows < n + s, y, 0.0)


def _rms_bwd(dh, xh, r, g):
    dxh = dh * g
    return r * (dxh - xh * jnp.mean(dxh * xh, axis=-1, keepdims=True))


def _matmul(a, b, mode, out_dtype, name, tm=512, tn=1024, tk=2048):
    if mode == "nn":
        (m, k), (k2, n) = a.shape, b.shape
    elif mode == "nt":
        (m, k), (n, k2) = a.shape, b.shape
    else:
        (k, m), (k2, n) = a.shape, b.shape
    assert k == k2
    tm, tn, tk = min(tm, m), min(tn, n), min(tk, k)
    assert m % tm == 0 and n % tn == 0 and k % tk == 0
    nk = k // tk
    dot = {"nn": _dot, "nt": _dot_nt, "tn": _dot_tn}[mode]

    def body(a_ref, b_ref, o_ref, acc):
        kk = pl.program_id(2)
        part = dot(a_ref[...].astype(BF16), b_ref[...].astype(BF16))
        if nk == 1:
            o_ref[...] = part.astype(out_dtype)
            return

        @pl.when(kk == 0)
        def _():
            acc[...] = part

        @pl.when(kk > 0)
        def _():
            acc[...] += part

        @pl.when(kk == nk - 1)
        def _():
            o_ref[...] = acc[...].astype(out_dtype)

    if mode == "tn":
        a_spec = pl.BlockSpec((tk, tm), lambda i, j, kk: (kk, i))
    else:
        a_spec = pl.BlockSpec((tm, tk), lambda i, j, kk: (i, kk))
    if mode == "nt":
        b_spec = pl.BlockSpec((tn, tk), lambda i, j, kk: (j, kk))
    else:
        b_spec = pl.BlockSpec((tk, tn), lambda i, j, kk: (kk, j))
    return pl.pallas_call(
        body, name=name,
        out_shape=jax.ShapeDtypeStruct((m, n), out_dtype),
        grid=(m // tm, n // tn, nk),
        in_specs=[a_spec, b_spec],
        out_specs=pl.BlockSpec((tm, tn), lambda i, j, kk: (i, j)),
        scratch_shapes=[pltpu.VMEM((tm, tn) if nk > 1 else (8, LANES), F32)],
        compiler_params=_params(dimension_semantics=("parallel", "parallel", "arbitrary")),
    )(a, b)


def _in_proj(x, g1, w_in_t, b_in, after):
    tm = 512

    def body(x_ref, g_ref, w_hbm, b_ref, after_ref, qkv_ref, uy_ref, gg_ref, h_ref, w):
        @pl.when(pl.program_id(0) == 0)
        def _():
            pltpu.sync_copy(w_hbm, w)

        xv = x_ref[...]
        r = lax.rsqrt(jnp.mean(xv * xv, axis=-1, keepdims=True) + EPS)
        h = ((xv * r) * g_ref[...]).astype(BF16)
        h_ref[...] = h
        row0 = 0
        for ref in (qkv_ref, uy_ref, gg_ref):
            for c0 in range(0, ref.shape[1], TILE):
                z = _dot_nt(h, w[row0:row0 + TILE, :]) + b_ref[:, row0:row0 + TILE]
                ref[:, c0:c0 + TILE] = z.astype(ref.dtype)
                row0 += TILE

    tok = lambda width: pl.BlockSpec((tm, width), lambda i: (i, 0))
    return pl.pallas_call(
        body, name="in_proj",
        out_shape=(jax.ShapeDtypeStruct((T, 3 * D_ATT), BF16),
                   jax.ShapeDtypeStruct((T, 2 * D_REC), F32),
                   jax.ShapeDtypeStruct((T, 2 * D), F32),
                   jax.ShapeDtypeStruct((T, D), BF16)),
        grid=(T // tm,),
        in_specs=[tok(D), pl.BlockSpec((1, D), lambda i: (0, 0)), pl.BlockSpec(memory_space=pl.ANY),
                  pl.BlockSpec((1, D_IN), lambda i: (0, 0)), pl.BlockSpec(memory_space=pl.ANY)],
        out_specs=(tok(3 * D_ATT), tok(2 * D_REC), tok(2 * D), tok(D)),
        scratch_shapes=[pltpu.VMEM((D_IN, D), BF16)],
        compiler_params=_params(dimension_semantics=("arbitrary",)),
    )(x, g1, w_in_t, b_in, after)


def _dz_specs(rows, tile_of, row_of):
    def spec(off, n, per_plane):
        def index(*ids):
            t = jnp.clip(tile_of(*ids) - off, 0, n - 1)
            return (t // per_plane, row_of(*ids), t % per_plane)
        return pl.BlockSpec((1, rows, TILE), index)
    return [spec(off, n, per) for off, n, per in DZ_ARRAYS]


def _dh_norm1_bwd(dz, w_in_t, x, g1, dx1, after):
    tm = 512

    def body(dqkv_ref, duy_ref, dgg_ref, w_hbm, x_ref, g_ref, dx1_ref, after_ref, gx_ref, dg_ref, w):
        @pl.when(pl.program_id(0) == 0)
        def _():
            pltpu.sync_copy(w_hbm, w)
            dg_ref[...] = jnp.zeros_like(dg_ref)

        dh, row0 = None, 0
        for ref in (dqkv_ref, duy_ref, dgg_ref):
            for plane in range(ref.shape[0]):
                cols = ref.shape[2]
                part = _dot(ref[plane], w[row0:row0 + cols, :])
                dh = part if dh is None else dh + part
                row0 += cols
        xv = x_ref[...]
        r = lax.rsqrt(jnp.mean(xv * xv, axis=-1, keepdims=True) + EPS)
        xh = xv * r
        dg_ref[...] += jnp.sum(dh * xh, axis=0, keepdims=True)
        gx_ref[...] = dx1_ref[...] + _rms_bwd(dh, xh, r, g_ref[...])

    tok = pl.BlockSpec((tm, D), lambda i: (i, 0))
    vec = pl.BlockSpec((1, D), lambda i: (0, 0))
    planes = lambda a: pl.BlockSpec((a.shape[0], tm, a.shape[2]), lambda i: (0, i, 0))
    return pl.pallas_call(
        body, name="dh_norm1_bwd",
        out_shape=(jax.ShapeDtypeStruct((T, D), F32), jax.ShapeDtypeStruct((1, D), F32)),
        grid=(T // tm,),
        in_specs=[planes(a) for a in dz] + [pl.BlockSpec(memory_space=pl.ANY), tok, vec, tok,
                                            pl.BlockSpec(memory_space=pl.ANY)],
        out_specs=(tok, vec),
        scratch_shapes=[pltpu.VMEM((D_IN, D), BF16)],
        compiler_params=_params(dimension_semantics=("arbitrary",)),
    )(*dz, w_in_t, x, g1, dx1, after)


def _grad_w_in(dz, h):
    def body(*refs):
        seg_refs = refs[:3]
        h_ref, gw_ref, gb_ref = refs[3:]
        j = pl.program_id(0)

        for s, (off, n, _) in enumerate(DZ_ARRAYS):
            @pl.when((j >= off) & (j < off + n))
            def _(s=s):
                a = seg_refs[s][0]
                gw_ref[...] = _dot_tn(a, h_ref[...]).astype(BF16)
                gb_ref[...] = jnp.sum(a.astype(F32), axis=0, keepdims=True)

    return pl.pallas_call(
        body, name="grad_w_in",
        out_shape=(jax.ShapeDtypeStruct((D_IN, D), BF16), jax.ShapeDtypeStruct((1, D_IN), F32)),
        grid=(N_DZ_TILES,),
        in_specs=_dz_specs(T, lambda j: j, lambda j: 0) + [pl.BlockSpec((T, D), lambda j: (0, 0))],
        out_specs=(pl.BlockSpec((TILE, D), lambda j: (j, 0)), pl.BlockSpec((1, TILE), lambda j: (0, j))),
        compiler_params=_params(dimension_semantics=("parallel",)),
    )(*dz, h)


def _rpb_rows(rpb):
    padded = jnp.pad(rpb, ((0, 0), (0, 0), (0, GRID_W - N_RPB_C)))
    rows = [padded[:, WIN_H - 1 - oi: 2 * WIN_H - 1 - oi].reshape(N_HEADS // HG, HG, KWIN)
            for oi in range(WIN_H)]
    return jnp.stack(rows, axis=0)


SKEW = KWIN - (WIN_W - 1)


MASKED = -1e30


def _bias_tiles(rows_ref, valid, bias_s):
    for oi in range(WIN_H):
        for hh in range(HG):
            row = jnp.broadcast_to(rows_ref[oi, 0, hh:hh + 1, :], (GRID_W, KWIN))
            tile = pltpu.roll(row, SKEW, 1, stride=1, stride_axis=0)
            bias_s[oi, hh * GRID_W:(hh + 1) * GRID_W, :] = jnp.where(valid, tile, MASKED)


def _bias_tile_grads(gb_s, flip, out_ref):
    for oi in range(WIN_H):
        for hh in range(HG):
            g = _dot_exact(flip, gb_s[oi, hh * GRID_W:(hh + 1) * GRID_W, :])
            back = pltpu.roll(g, KWIN - (GRID_W - WIN_W), 1, stride=1, stride_axis=0)
            out_ref[0, oi, hh:hh + 1, :] = jnp.sum(back, axis=0, keepdims=True)


def _rpb_fold(row_grads):
    g = row_grads.transpose(1, 0, 2, 3).reshape(WIN_H, N_HEADS, WIN_H, GRID_W)
    g = g.transpose(0, 2, 1, 3)

    def body(g_ref, o_ref):
        for dr in range(N_RPB_R):
            terms = [g_ref[oi, i] for oi in range(WIN_H) for i in range(WIN_H) if i - oi + WIN_H - 1 == dr]
            acc = terms[0]
            for term in terms[1:]:
                acc = acc + term
            o_ref[dr] = acc

    out = pl.pallas_call(
        body, name="rpb_fold",
        out_shape=jax.ShapeDtypeStruct((N_RPB_R, N_HEADS, GRID_W), F32),
    )(g)
    return out.transpose(1, 0, 2)[:, :, :N_RPB_C]


ATT_GROUPS = N_HEADS // HG
ATT_UNROLL = 8


def _stacked(rows64, same_head):
    return jnp.where(same_head, jnp.concatenate([rows64] * HG, axis=0), jnp.zeros((), BF16))


def _own_heads(stacked):
    head = lax.broadcasted_iota(jnp.int32, (GRID_W, HC), 1) // DH
    out = stacked[:GRID_W]
    for h in range(1, HG):
        out = jnp.where(head == h, stacked[h * GRID_W:(h + 1) * GRID_W], out)
    return out


def _att_scores(q_ref, k_ref, bias_ref, same_head, r):
    rs = jnp.clip(r - WIN_H // 2, 0, ROWS - WIN_H)
    oi = r - rs
    q0 = pl.multiple_of(r * GRID_W, GRID_W)
    k0 = pl.multiple_of(rs * GRID_W, GRID_W)
    q2 = _stacked(q_ref[pl.ds(q0, GRID_W), :] * (DH ** -0.5), same_head)
    kw = k_ref[pl.ds(k0, KWIN), :]
    s = _dot_nt(q2, kw) + bias_ref[oi]
    e = jnp.exp(s - jnp.max(s, axis=-1, keepdims=True))
    return e, 1.0 / jnp.sum(e, axis=-1, keepdims=True), q2, kw, q0, k0, oi


def _att_specs():
    col = lambda off: pl.BlockSpec((T, HC), lambda g: (0, g + off * ATT_GROUPS))
    tables = [pl.BlockSpec((WIN_H, 1, HG, KWIN), lambda g: (0, g, 0, 0)),
              pl.BlockSpec((GRID_W, KWIN), lambda g: (0, 0)),
              pl.BlockSpec((HQ, HC), lambda g: (0, 0))]
    return col, tables, pltpu.VMEM((WIN_H, HQ, KWIN), F32)


def _att_fwd(qkv, bias_rows, after):
    valid_np, same_head_np = _att_tables()

    def body(q_ref, k_ref, v_ref, rows_ref, valid_ref, head_ref, after_ref, o_ref, bias_s):
        same_head = head_ref[...] > 0.5
        _bias_tiles(rows_ref, valid_ref[...] > 0.5, bias_s)

        def row(r, carry):
            e, rl, _, _, q0, k0, _ = _att_scores(q_ref, k_ref, bias_s, same_head, r)
            o2 = _dot((e * rl).astype(BF16), v_ref[pl.ds(k0, KWIN), :])
            o_ref[pl.ds(q0, GRID_W), :] = _own_heads(o2).astype(BF16)
            return carry

        lax.fori_loop(0, ROWS, row, 0, unroll=ATT_UNROLL)

    col, tables, tiles = _att_specs()
    return pl.pallas_call(
        body, name="att_fwd",
        out_shape=jax.ShapeDtypeStruct((T, D_ATT), BF16),
        grid=(ATT_GROUPS,),
        in_specs=[col(0), col(1), col(2)] + tables + [pl.BlockSpec(memory_space=pl.ANY)],
        out_specs=col(0),
        scratch_shapes=[tiles],
        compiler_params=_params(dimension_semantics=("parallel",)),
    )(qkv, qkv, qkv, bias_rows, jnp.asarray(valid_np), jnp.asarray(same_head_np), after)


def _att_bwd(qkv, bias_rows, datt, after):
    valid_np, same_head_np = _att_tables()

    def body(q_ref, k_ref, v_ref, do_ref, rows_ref, valid_ref, head_ref, flip_ref, after_ref,
             dqkv_ref, grows_ref, dk_acc, dv_acc, bias_s, gb_s):
        same_head = head_ref[...] > 0.5
        dk_acc[...] = jnp.zeros_like(dk_acc)
        dv_acc[...] = jnp.zeros_like(dv_acc)
        gb_s[...] = jnp.zeros_like(gb_s)
        _bias_tiles(rows_ref, valid_ref[...] > 0.5, bias_s)

        def row(r, carry):
            e, rl, q2, kw, q0, k0, oi = _att_scores(q_ref, k_ref, bias_s, same_head, r)
            do2 = _stacked(do_ref[pl.ds(q0, GRID_W), :], same_head)
            vw = v_ref[pl.ds(k0, KWIN), :]
            p = e * rl
            dp = _dot_nt(do2, vw)
            ds = p * (dp - jnp.sum(dp * p, axis=-1, keepdims=True))
            p16 = p.astype(BF16)
            ds16 = ds.astype(BF16)
            dv_acc[pl.ds(k0, KWIN), :] += _dot_tn(p16, do2)
            dk_acc[pl.ds(k0, KWIN), :] += _dot_tn(ds16, q2)
            dq2 = _dot(ds16, kw) * (DH ** -0.5)
            dqkv_ref[0, pl.ds(q0, GRID_W), :] = _own_heads(dq2).astype(BF16)
            gb_s[oi] += ds
            return carry

        lax.fori_loop(0, ROWS, row, 0, unroll=ATT_UNROLL)
        dqkv_ref[1] = dk_acc[...].astype(BF16)
        dqkv_ref[2] = dv_acc[...].astype(BF16)
        _bias_tile_grads(gb_s, flip_ref[...], grows_ref)

    col, tables, tiles = _att_specs()
    return pl.pallas_call(
        body, name="att_bwd",
        out_shape=(jax.ShapeDtypeStruct((3, T, D_ATT), BF16),
                   jax.ShapeDtypeStruct((ATT_GROUPS, WIN_H, HG, KWIN), F32)),
        grid=(ATT_GROUPS,),
        in_specs=[col(0), col(1), col(2), col(0)] + tables + [pl.BlockSpec((GRID_W, GRID_W), lambda g: (0, 0)),
                                                              pl.BlockSpec(memory_space=pl.ANY)],
        out_specs=(pl.BlockSpec((3, T, HC), lambda g: (0, 0, g)),
                   pl.BlockSpec((1, WIN_H, HG, KWIN), lambda g: (g, 0, 0, 0))),
        scratch_shapes=[pltpu.VMEM((T, HC), F32), pltpu.VMEM((T, HC), F32), tiles, tiles],
        compiler_params=_params(dimension_semantics=("parallel",)),
    )(qkv, qkv, qkv, datt, bias_rows, jnp.asarray(valid_np), jnp.asarray(same_head_np),
      jnp.asarray(np.eye(GRID_W, dtype=np.float32)[::-1].copy()), after)


def _conv_taps(up):
    return (_shift_rows(up, 2), _shift_rows(up, 1), up, _shift_rows(up, -1))


def _pair_block_diag(w_pair, dup, same_half):
    return jnp.where(same_half, _dot(w_pair.astype(BF16), dup), 0.0).astype(BF16)


def _gates(u, u16, wa, ba, wi, bi, lam):
    r = _sigmoid(_dot(u16, wa) + ba)
    ig = _sigmoid(_dot(u16, wi) + bi)
    sp = _softplus(-lam)
    log_a = (-LRU_C) * r * sp
    a = jnp.exp(log_a)
    mult2 = jnp.maximum(_one_minus_square(log_a, a), 0.0)
    return r, ig, sp, a, jnp.sqrt(mult2), mult2


SCAN_BLOCKS = 8


def _scans(jobs):
    c = jobs[0][0].shape[1]
    nblk = T // 8
    rows = lax.broadcasted_iota(jnp.int32, (8, c), 0)

    def block(a, b, reverse):
        for s in (1, 2, 4):
            if reverse:
                keep = rows < 8 - s
                a_s = jnp.where(keep, pltpu.roll(a, 8 - s, 0), 1.0)
                b_s = jnp.where(keep, pltpu.roll(b, 8 - s, 0), 0.0)
            else:
                keep = rows >= s
                a_s = jnp.where(keep, pltpu.roll(a, s, 0), 1.0)
                b_s = jnp.where(keep, pltpu.roll(b, s, 0), 0.0)
            b = a * b_s + b
            a = a * a_s
        return a, b

    def step(i, carry):
        out = []
        for (a_ref, b_ref, h_ref, reverse), h_prev in zip(jobs, carry):
            for u in range(SCAN_BLOCKS):
                blk = i * SCAN_BLOCKS + u
                if reverse:
                    blk = nblk - 1 - blk
                t0 = pl.multiple_of(blk * 8, 8)
                a, b = block(a_ref[pl.ds(t0, 8), :], b_ref[pl.ds(t0, 8), :], reverse)
                h = a * h_prev + b
                h_ref[pl.ds(t0, 8), :] = h
                h_prev = jnp.broadcast_to(h[0:1] if reverse else h[7:8], (8, c))
            out.append(h_prev)
        return tuple(out)

    lax.fori_loop(0, nblk // SCAN_BLOCKS, step, tuple(jnp.zeros((8, c), F32) for _ in jobs))


def _rec_specs():
    tok = lambda off: pl.BlockSpec((T, CG), lambda g: (0, g + off))
    per_ch = lambda rows: pl.BlockSpec((rows, CG), lambda g: (0, g))
    wspec = pl.BlockSpec((2, 1, CG, REC_BLOCK), lambda g: (0, g, 0, 0))
    const = lambda shape: pl.BlockSpec(shape, lambda g: (0, 0))
    return tok, per_ch, wspec, const


def _rec_fwd(uy, conv_w, conv_b, w_a, b_a, w_i, b_i, lam):
    tok, per_ch, wspec, const = _rec_specs()

    def body(up_ref, yb_ref, cw_ref, cb_ref, wa_ref, ba_ref, wi_ref, bi_ref, lam_ref, dup_ref, half_ref,
             hf_ref, hb_ref, yrec_ref, am_ref, bx_f, bx_b):
        dup = dup_ref[...]
        same_half = half_ref[...] > 0.5
        taps = _conv_taps(up_ref[...])
        u = cb_ref[...]
        for j in range(4):
            u = u + taps[j] * cw_ref[j:j + 1, :]
        u16 = u.astype(BF16)
        for d, bx_s in enumerate((bx_f, bx_b)):
            wa = _pair_block_diag(wa_ref[d, 0], dup, same_half)
            wi = _pair_block_diag(wi_ref[d, 0], dup, same_half)
            _, ig, _, a, mult, _ = _gates(u, u16, wa, ba_ref[d:d + 1, :], wi, bi_ref[d:d + 1, :],
                                       lam_ref[d:d + 1, :])
            am_ref[2 * d] = a
            am_ref[2 * d + 1] = mult
            bx_s[...] = mult * (ig * u)
        _scans([(am_ref.at[0], bx_f, hf_ref, False), (am_ref.at[2], bx_b, hb_ref, True)])
        gelu, _ = _gelu_and_grad(yb_ref[...])
        yrec_ref[...] = ((hf_ref[...] + hb_ref[...]) * gelu).astype(BF16)

    return pl.pallas_call(
        body, name="rec_fwd",
        out_shape=(jax.ShapeDtypeStruct((T, D_REC), F32), jax.ShapeDtypeStruct((T, D_REC), F32),
                   jax.ShapeDtypeStruct((T, D_REC), BF16), jax.ShapeDtypeStruct((4, T, D_REC), F32)),
        grid=(N_CG,),
        in_specs=[tok(0), tok(N_CG), per_ch(4), per_ch(1), wspec, per_ch(2), wspec, per_ch(2), per_ch(2),
                  const((REC_BLOCK, CG)), const((CG, CG))],
        out_specs=(tok(0), tok(0), tok(0), pl.BlockSpec((4, T, CG), lambda g: (0, 0, g))),
        scratch_shapes=[pltpu.VMEM((T, CG), F32)] * 2,
        compiler_params=_params(dimension_semantics=("parallel",)),
    )(uy, uy, conv_w, conv_b, w_a, b_a, w_i, b_i, lam,
      jnp.asarray(_dup_table(), BF16), jnp.asarray(_pair_mask()))


def _rec_bwd(uy, hf, hb, am, dyrec, conv_w, conv_b, w_a, b_a, w_i, b_i, lam, after):
    tok, per_ch, wspec, const = _rec_specs()

    def body(up_ref, yb_ref, hf_ref, hb_ref, am_ref, dy_ref, cw_ref, cb_ref, wa_ref, ba_ref, wi_ref, bi_ref,
             lam_ref, dup_ref, dupt_ref, half_ref, after_ref,
             duy_ref, dcw_ref, dcb_ref, dwa_ref, dba_ref, dwi_ref, dbi_ref, dlam_ref,
             a_s0, a_s1, dh_s, g_s0, g_s1):
        dup = dup_ref[...]
        dup_t = dupt_ref[...]
        same_half = half_ref[...] > 0.5
        taps = _conv_taps(up_ref[...])
        u = cb_ref[...]
        for j in range(4):
            u = u + taps[j] * cw_ref[j:j + 1, :]
        u16 = u.astype(BF16)
        gelu, dgelu = _gelu_and_grad(yb_ref[...])
        dy = dy_ref[...]
        duy_ref[1] = (dy * (hf_ref[...] + hb_ref[...]) * dgelu).astype(BF16)
        dh_s[...] = dy * gelu
        a_s0[...] = _shift_rows(am_ref[0], -1)
        a_s1[...] = _shift_rows(am_ref[2], 1)
        _scans([(a_s0, dh_s, g_s0, True), (a_s1, dh_s, g_s1, False)])
        du = jnp.zeros((T, CG), F32)
        for d, g_s in enumerate((g_s0, g_s1)):
            reverse = d == 1
            wa = _pair_block_diag(wa_ref[d, 0], dup, same_half)
            wi = _pair_block_diag(wi_ref[d, 0], dup, same_half)
            lam_d = lam_ref[d:d + 1, :]
            r = _sigmoid(_dot(u16, wa) + ba_ref[d:d + 1, :])
            ig = _sigmoid(_dot(u16, wi) + bi_ref[d:d + 1, :])
            sp = _softplus(-lam_d)
            a, mult = am_ref[2 * d], am_ref[2 * d + 1]
            mult2 = mult * mult
            g = g_s[...]
            h_prev = _shift_rows(hb_ref[...], -1) if reverse else _shift_rows(hf_ref[...], 1)
            da = g * h_prev
            dmult = g * (ig * u)
            dig = g * mult * u
            du = du + g * mult * ig
            dmult_dlog = jnp.where(mult2 > 0.0, -(a * a) * lax.rsqrt(mult2), 0.0)
            dlog_a = da * a + dmult * dmult_dlog
            dr = dlog_a * ((-LRU_C) * sp)
            dsp = jnp.sum(dlog_a * ((-LRU_C) * r), axis=0, keepdims=True)
            dlam_ref[d:d + 1, :] = dsp * (-_sigmoid(-lam_d))
            dga = dr * r * (1.0 - r)
            dgi = dig * ig * (1.0 - ig)
            dga16 = dga.astype(BF16)
            dgi16 = dgi.astype(BF16)
            du = du + _dot_nt(dga16, wa) + _dot_nt(dgi16, wi)
            dwa_ref[d, 0] = _dot_exact(jnp.where(same_half, _dot_tn(u16, dga16), 0.0), dup_t)
            dwi_ref[d, 0] = _dot_exact(jnp.where(same_half, _dot_tn(u16, dgi16), 0.0), dup_t)
            dba_ref[d:d + 1, :] = jnp.sum(dga, axis=0, keepdims=True)
            dbi_ref[d:d + 1, :] = jnp.sum(dgi, axis=0, keepdims=True)
        dcb_ref[...] = jnp.sum(du, axis=0, keepdims=True)
        for j in range(4):
            dcw_ref[j:j + 1, :] = jnp.sum(du * taps[j], axis=0, keepdims=True)
        dup_in = (_shift_rows(du, -2) * cw_ref[0:1, :] + _shift_rows(du, -1) * cw_ref[1:2, :]
                  + du * cw_ref[2:3, :] + _shift_rows(du, 1) * cw_ref[3:4, :])
        duy_ref[0] = dup_in.astype(BF16)

    wshape = jax.ShapeDtypeStruct((2, N_CG, CG, REC_BLOCK), F32)
    vec = lambda rows: jax.ShapeDtypeStruct((rows, D_REC), F32)
    dup_np = _dup_table()
    return pl.pallas_call(
        body, name="rec_bwd",
        out_shape=(jax.ShapeDtypeStruct((2, T, D_REC), BF16),
                   vec(4), vec(1), wshape, vec(2), wshape, vec(2), vec(2)),
        grid=(N_CG,),
        in_specs=[tok(0), tok(N_CG), tok(0), tok(0), pl.BlockSpec((4, T, CG), lambda g: (0, 0, g)), tok(0),
                  per_ch(4), per_ch(1), wspec, per_ch(2), wspec, per_ch(2), per_ch(2),
                  const((REC_BLOCK, CG)), const((CG, REC_BLOCK)), const((CG, CG)),
                  pl.BlockSpec(memory_space=pl.ANY)],
        out_specs=(pl.BlockSpec((2, T, CG), lambda g: (0, 0, g)),
                   per_ch(4), per_ch(1), wspec, per_ch(2), wspec, per_ch(2), per_ch(2)),
        scratch_shapes=[pltpu.VMEM((T, CG), F32)] * 5,
        compiler_params=_params(dimension_semantics=("parallel",)),
    )(uy, uy, hf, hb, am, dyrec, conv_w, conv_b, w_a, b_a, w_i, b_i, lam,
      jnp.asarray(dup_np, BF16), jnp.asarray(dup_np.T.copy()), jnp.asarray(_pair_mask()), after)


TM_MIX = 256


def _mix_specs():
    tok = lambda width, blk=0: pl.BlockSpec((TM_MIX, width), lambda i: (i, blk))
    full = lambda shape: pl.BlockSpec(shape, lambda i: (0, 0))
    return tok, full


def _mix_fwd(x, att, yrec, gg, w_att_o_t, w_rec_o, w_out):
    tok, full = _mix_specs()

    def body(x_ref, att_ref, yr_ref, ga_ref, gr_ref, wao_ref, wro_ref, wo_ref, x1_ref, mixed_ref):
        y_att = _dot_nt(att_ref[...], wao_ref[...])
        y_rec = _dot(yr_ref[...], wro_ref[...])
        mixed = (_sigmoid(ga_ref[...]) * y_att + _sigmoid(gr_ref[...]) * y_rec).astype(BF16)
        mixed_ref[...] = mixed
        x1_ref[...] = x_ref[...] + _dot(mixed, wo_ref[...])

    return pl.pallas_call(
        body, name="mix_fwd",
        out_shape=(jax.ShapeDtypeStruct((T, D), F32), jax.ShapeDtypeStruct((T, D), BF16)),
        grid=(T // TM_MIX,),
        in_specs=[tok(D), tok(D_ATT), tok(D_REC), tok(D, 0), tok(D, 1),
                  full((D, D_ATT)), full((D_REC, D)), full((D, D))],
        out_specs=(tok(D), tok(D)),
        compiler_params=_params(dimension_semantics=("parallel",)),
    )(x, att, yrec, gg, gg, w_att_o_t, w_rec_o, w_out)


def _mix_bwd(dx1, att, yrec, gg, w_att_o_t, w_rec_o, w_out, after):
    tok, full = _mix_specs()

    def body(dx_ref, att_ref, yr_ref, ga_ref, gr_ref, wao_ref, wro_ref, wo_ref, after_ref,
             dgg_ref, dya_ref, dyr_ref, datt_ref, dyrp_ref):
        dmixed = _dot_nt(dx_ref[...].astype(BF16), wo_ref[...])
        y_att = _dot_nt(att_ref[...], wao_ref[...])
        y_rec = _dot(yr_ref[...], wro_ref[...])
        sa = _sigmoid(ga_ref[...])
        sr = _sigmoid(gr_ref[...])
        dgg_ref[0] = (dmixed * y_att * sa * (1.0 - sa)).astype(BF16)
        dgg_ref[1] = (dmixed * y_rec * sr * (1.0 - sr)).astype(BF16)
        dya = (dmixed * sa).astype(BF16)
        dyr = (dmixed * sr).astype(BF16)
        dya_ref[...] = dya
        dyr_ref[...] = dyr
        datt_ref[...] = _dot(dya, wao_ref[...]).astype(BF16)
        dyrp_ref[...] = _dot_nt(dyr, wro_ref[...])

    return pl.pallas_call(
        body, name="mix_bwd",
        out_shape=(jax.ShapeDtypeStruct((2, T, D), BF16),
                   jax.ShapeDtypeStruct((T, D), BF16), jax.ShapeDtypeStruct((T, D), BF16),
                   jax.ShapeDtypeStruct((T, D_ATT), BF16), jax.ShapeDtypeStruct((T, D_REC), F32)),
        grid=(T // TM_MIX,),
        in_specs=[tok(D), tok(D_ATT), tok(D_REC), tok(D, 0), tok(D, 1),
                  full((D, D_ATT)), full((D_REC, D)), full((D, D)), pl.BlockSpec(memory_space=pl.ANY)],
        out_specs=(pl.BlockSpec((2, TM_MIX, D), lambda i: (0, i, 0)),
                   tok(D), tok(D), tok(D_ATT), tok(D_REC)),
        compiler_params=_params(dimension_semantics=("parallel",)),
    )(dx1, att, yrec, gg, gg, w_att_o_t, w_rec_o, w_out, after)


TM_FFN = 256
FF_CHUNK = 1024


def _ffn_loss(x1, target, g2, gf, w_ff1_t, w_ff2):
    n_chunks = D_FF // FF_CHUNK

    def body(x1_ref, tg_ref, g2_ref, gf_ref, w1_hbm, w2_hbm,
             loss_ref, dx1_ref, h2_ref, act_ref, dpre_ref, dx2_ref, dg2_ref, dgf_ref,
             w1, w2, relu_s):
        i = pl.program_id(0)

        @pl.when(i == 0)
        def _():
            pltpu.sync_copy(w1_hbm, w1)
            pltpu.sync_copy(w2_hbm, w2)
            loss_ref[...] = jnp.zeros_like(loss_ref)
            dg2_ref[...] = jnp.zeros_like(dg2_ref)
            dgf_ref[...] = jnp.zeros_like(dgf_ref)

        x1v = x1_ref[...]
        r2 = lax.rsqrt(jnp.mean(x1v * x1v, axis=-1, keepdims=True) + EPS)
        xh2 = x1v * r2
        h2 = (xh2 * g2_ref[...]).astype(BF16)
        h2_ref[...] = h2
        x2 = x1v
        for c in range(n_chunks):
            ff = slice(c * FF_CHUNK, (c + 1) * FF_CHUNK)
            rl = jnp.maximum(_dot_nt(h2, w1[ff, :]), 0.0)
            relu_s[:, ff] = rl
            act = (rl * rl).astype(BF16)
            act_ref[:, ff] = act
            x2 = x2 + _dot(act, w2[ff, :])
        r3 = lax.rsqrt(jnp.mean(x2 * x2, axis=-1, keepdims=True) + EPS)
        xh3 = x2 * r3
        err = xh3 * gf_ref[...] - tg_ref[...]
        loss_ref[...] += 0.5 * jnp.sum(jnp.mean(err * err, axis=-1, keepdims=True))
        dy = err * (1.0 / D)
        dgf_ref[...] += jnp.sum(dy * xh3, axis=0, keepdims=True)
        dx2 = _rms_bwd(dy, xh3, r3, gf_ref[...])
        dx2_16 = dx2.astype(BF16)
        dx2_ref[...] = dx2_16
        dh2 = jnp.zeros((TM_FFN, D), F32)
        for c in range(n_chunks):
            ff = slice(c * FF_CHUNK, (c + 1) * FF_CHUNK)
            dpre = (_dot_nt(dx2_16, w2[ff, :]) * (2.0 * relu_s[:, ff])).astype(BF16)
            dpre_ref[:, ff] = dpre
            dh2 = dh2 + _dot(dpre, w1[ff, :])
        dg2_ref[...] += jnp.sum(dh2 * xh2, axis=0, keepdims=True)
        dx1_ref[...] = dx2 + _rms_bwd(dh2, xh2, r2, g2_ref[...])

    tok = lambda width: pl.BlockSpec((TM_FFN, width), lambda i: (i, 0))
    vec = pl.BlockSpec((1, D), lambda i: (0, 0))
    hbm = pl.BlockSpec(memory_space=pl.ANY)
    return pl.pallas_call(
        body, name="ffn_loss",
        out_shape=(jax.ShapeDtypeStruct((8, 128), F32), jax.ShapeDtypeStruct((T, D), F32),
                   jax.ShapeDtypeStruct((T, D), BF16), jax.ShapeDtypeStruct((T, D_FF), BF16),
                   jax.ShapeDtypeStruct((T, D_FF), BF16), jax.ShapeDtypeStruct((T, D), BF16),
                   jax.ShapeDtypeStruct((1, D), F32), jax.ShapeDtypeStruct((1, D), F32)),
        grid=(T // TM_FFN,),
        in_specs=[tok(D), tok(D), vec, vec, hbm, hbm],
        out_specs=(pl.BlockSpec((8, 128), lambda i: (0, 0)), tok(D), tok(D), tok(D_FF), tok(D_FF), tok(D),
                   vec, vec),
        scratch_shapes=[pltpu.VMEM((D_FF, D), BF16), pltpu.VMEM((D_FF, D), BF16),
                        pltpu.VMEM((TM_FFN, D_FF), F32)],
        compiler_params=_params(dimension_semantics=("arbitrary",)),
    )(x1, target, g2, gf, w_ff1_t, w_ff2)


def _local_step(x, target, p, late_weights, late_weights_ready, reduce_first, reduce_early, reduce_late):
    bias = _rpb_rows(p["rpb"])
    pairs = lambda w: w.reshape(2, N_CG, CG, REC_BLOCK)
    w_a, w_i = pairs(p["w_rg_a"]), pairs(p["w_rg_i"])
    rec_params = (p["conv_w"], p["conv_b"], w_a, p["b_rg_a"], w_i, p["b_rg_i"], p["lru_lambda"])

    qkv, uy, gg, h = _in_proj(x, p["ln1_g"], p["w_in_t"], p["b_in"], p["later_weights_started"])
    hf, hb, yrec, am = _rec_fwd(uy, *rec_params)
    att = _att_fwd(qkv, bias, late_weights(yrec, 0))
    p = {**p, **late_weights_ready(late_weights(att, 1), 0)}
    x1, mixed = _mix_fwd(x, att, yrec, gg, p["w_att_o_t"], p["w_rec_o"], p["w_out"])
    p = {**p, **late_weights_ready(x1, 1)}
    loss8, dx1, h2, act, dpre, dx2, g_ln2, g_lnf = _ffn_loss(
        x1, target, p["ln2_g"], p["lnf_g"], p["w_ff1_t"], p["w_ff2"])

    grads = {"ln2_g": g_ln2, "lnf_g": g_lnf,
             "w_ff1_t": _matmul(dpre, h2, "tn", BF16, "g_w_ff1"),
             "w_ff2": _matmul(act, dx2, "tn", BF16, "g_w_ff2")}
    dgg, dya, dyr, datt, dyrp = _mix_bwd(dx1, att, yrec, gg, p["w_att_o_t"], p["w_rec_o"], p["w_out"],
                                         reduce_first(grads, None))
    duy, g_cw, g_cb, g_wa, g_ba, g_wi, g_bi, g_lam = _rec_bwd(uy, hf, hb, am, dyrp, *rec_params,
                                                              reduce_first(None, dgg))
    blocks = lambda g: g.reshape(2, N_REC_BLOCKS, REC_BLOCK, REC_BLOCK)
    grads.update({
        "w_att_o_t": _matmul(dya, att, "tn", BF16, "g_w_att_o"),
        "conv_w": g_cw, "conv_b": g_cb, "w_rg_a": blocks(g_wa), "b_rg_a": g_ba,
        "w_rg_i": blocks(g_wi), "b_rg_i": g_bi, "lru_lambda": g_lam,
        "w_rec_o": _matmul(yrec, dyr, "tn", BF16, "g_w_rec_o"),
        "w_out": _matmul(mixed, dx1, "tn", BF16, "g_w_out"),
    })
    dqkv, gbias = _att_bwd(qkv, bias, datt, reduce_early(grads))
    dz = (dqkv, duy, dgg)
    g_w_in_t, g_b_in = _grad_w_in(dz, h)
    grads.update(w_in_t=g_w_in_t, b_in=g_b_in)
    grad_x, g_ln1 = _dh_norm1_bwd(dz, p["w_in_t"], x, p["ln1_g"], dx1, reduce_late(grads))
    grads.update(ln1_g=g_ln1, rpb=_rpb_fold(gbias))
    return loss8[0:1, 0:1], grad_x, grads


MESH_ID = pl.DeviceIdType.MESH
ANY = pl.BlockSpec(memory_space=pl.ANY)

CHAN_BLOCK_ROWS = 32
GATE_ROWS = 2 * 2 * N_REC_BLOCKS * REC_BLOCK * REC_BLOCK // (N_DEV * D)
SECTIONS = (("w_in_t", 704, D), ("w_rec_o", 128, D), ("w_out", 128, D), ("w_ff1_t", 512, D),
            ("w_ff2", 512, D), ("chan", CHAN_BLOCK_ROWS, D), ("w_att_o_t", 128, D_ATT),
            ("gates", GATE_ROWS, D))
N_SEC = len(SECTIONS)
N_CHAN_ROWS = 10
CHAN = (("conv_w", 4), ("b_rg_a", 2), ("b_rg_i", 2), ("lru_lambda", 2))


def _position():
    return lax.axis_index("x"), lax.axis_index("y"), lax.axis_index("c")


def _other_chips(x, y):
    return [(1 - x, y), (x, 1 - y), (1 - x, 1 - y)]


PASS_ON_IDS, PAIR_EARLY_ID, PAIR_LATE_ID, PAIR_FIRST_ID, SMALL_PASS_ON_ID = (1, 4), 2, 3, 5, 6


def _pair_handshake(x, y, c):
    barrier = pltpu.get_barrier_semaphore()
    pl.semaphore_signal(barrier, inc=1, device_id=(x, y, 1 - c), device_id_type=MESH_ID)
    pl.semaphore_wait(barrier, 1)


def _block_of(ref, dev, rows):
    return ref.at[pl.ds(pl.multiple_of(dev * rows, 16), rows)]


def _all_gather(shards, name, pieces):
    ns = len(shards)
    parts = [(s, j * (a.shape[0] // n), a.shape[0] // n)
             for s, (a, n) in enumerate(zip(shards, pieces)) for j in range(n)]
    np_ = len(parts)

    def body(*refs):
        x_refs, out_refs = refs[:ns], refs[ns:2 * ns]
        send_sems, recv_sems, local_sems = refs[2 * ns:]
        x, y, c = _position()
        me, sibling = (x, y, c), (x, y, 1 - c)
        x_nbr, y_nbr, diagonal = _other_chips(x, y)
        north = c == 1
        relay_from = (jnp.where(north, x_nbr[0], y_nbr[0]), jnp.where(north, x_nbr[1], y_nbr[1]))
        relay_to = (jnp.where(north, y_nbr[0], x_nbr[0]), jnp.where(north, y_nbr[1], x_nbr[1]))

        def rows(v, px, py, pc):
            s, r0, r = parts[v]
            start = (4 * px + 2 * py + pc) * shards[s].shape[0] + r0
            return out_refs[s].at[pl.ds(pl.multiple_of(start, 16), r)]

        def own(v):
            s, r0, r = parts[v]
            return x_refs[s].at[pl.ds(r0, r)]

        def copy(k, v, block, to, from_shard=False):
            return pltpu.make_async_remote_copy(
                src_ref=own(v) if from_shard else rows(v, *block), dst_ref=rows(v, *block),
                send_sem=send_sems.at[k * np_ + v], recv_sem=recv_sems.at[k * np_ + v],
                device_id=to, device_id_type=MESH_ID)

        sections = range(np_)
        mine = [pltpu.make_async_copy(own(v), rows(v, *me), local_sems.at[v]) for v in sections]
        sent = [copy(k, v, me, to, True) for v in sections
                for k, to in enumerate((sibling, (*x_nbr, c), (*y_nbr, c)))]
        for cp in mine + sent:
            cp.start()
        for s in sections:
            copy(1, s, (*x_nbr, c), me).wait_recv()
            copy(2, s, (*y_nbr, c), me).wait_recv()
            sent += [copy(3, s, (*relay_from, c), (*relay_to, c)),
                     copy(4, s, (*x_nbr, c), sibling), copy(5, s, (*y_nbr, c), sibling)]
            for cp in sent[-3:]:
                cp.start()
        for s in sections:
            copy(3, s, (*diagonal, c), me).wait_recv()
            sent.append(copy(6, s, (*diagonal, c), sibling))
            sent[-1].start()
        for s in sections:
            copy(0, s, sibling, me).wait_recv()
            for k, chip in ((4, x_nbr), (5, y_nbr), (6, diagonal)):
                copy(k, s, (*chip, 1 - c), me).wait_recv()
        for cp in sent:
            cp.wait_send()
        for cp in mine:
            cp.wait()

    return pl.pallas_call(
        body, name=name,
        out_shape=tuple(jax.ShapeDtypeStruct((N_DEV * s.shape[0], s.shape[1]), s.dtype) for s in shards),
        in_specs=[ANY] * ns,
        out_specs=(ANY,) * ns,
        scratch_shapes=[pltpu.SemaphoreType.DMA((7 * np_,)), pltpu.SemaphoreType.DMA((7 * np_,)),
                        pltpu.SemaphoreType.DMA((np_,))],
    )(*shards)


HBM = pl.BlockSpec(memory_space=pltpu.HBM)
SEM = pl.BlockSpec(memory_space=pltpu.SEMAPHORE)
EFFECT = pltpu.SideEffectType.DATAFLOW_SIDE_EFFECTING


def _in_hbm(a):
    return pltpu.with_memory_space_constraint(a, pltpu.HBM)


def _first_hop_copies(rows, which, x_refs, zones, send_sems, recv_sems):
    ns = len(rows)
    x, y, c = _position()
    targets = [(x, y, 1 - c)] + [(cx, cy, c) for cx, cy in _other_chips(x, y)]
    return [pltpu.make_async_remote_copy(
        src_ref=x_refs[i], dst_ref=_block_of(zones[i], 4 * x + 2 * y + c, rows[s]),
        send_sem=send_sems.at[k * ns + s], recv_sem=recv_sems.at[k * ns + s],
        device_id=to, device_id_type=MESH_ID)
        for k, to in enumerate(targets) for i, s in enumerate(which)]


def _after_all(arrays, name):
    def body(*refs):
        refs[-1][...] = jnp.zeros_like(refs[-1])

    return pl.pallas_call(
        body, name=name,
        out_shape=jax.ShapeDtypeStruct((8, LANES), F32),
        in_specs=[pl.BlockSpec(memory_space=pl.ANY)] * len(arrays),
        out_specs=pl.BlockSpec(memory_space=pltpu.VMEM),
    )(*arrays)


def _own_blocks_placed(shards, after, name):
    ns = len(shards)
    x, y, c = _position()
    me = jnp.reshape(4 * x + 2 * y + c, (1,)).astype(jnp.int32)
    tokens = [] if after is None else [after]

    def body(me_ref, *refs):
        for s in range(ns):
            refs[ns + len(tokens) + s][...] = refs[s][...]

    return pl.pallas_call(
        body, name=name,
        out_shape=tuple(jax.ShapeDtypeStruct((N_DEV * s.shape[0], s.shape[1]), s.dtype) for s in shards),
        grid_spec=pltpu.PrefetchScalarGridSpec(
            num_scalar_prefetch=1, grid=(1,),
            in_specs=[pl.BlockSpec(s.shape, lambda i, me: (0, 0)) for s in shards] + [ANY] * len(tokens),
            out_specs=tuple(pl.BlockSpec(s.shape, lambda i, me: (me[0], 0)) for s in shards)),
        compiler_params=_params(dimension_semantics=("arbitrary",)),
    )(me, *shards, *tokens)


def _gather_start(shards, after, name):
    ns = len(shards)
    zones = _own_blocks_placed(shards, after, name + "_own_blocks")

    def body(*refs):
        for cp in _first_hop_copies([s.shape[0] for s in shards], range(ns), refs[:ns], refs[ns:2 * ns],
                                    refs[2 * ns], refs[2 * ns + 1]):
            cp.start()
        refs[-1][...] = jnp.zeros_like(refs[-1])

    out = pl.pallas_call(
        body, name=name,
        out_shape=(pltpu.SemaphoreType.DMA((4 * ns,)), pltpu.SemaphoreType.DMA((4 * ns,)),
                   *[pltpu.HBM(a.shape, a.dtype) for a in (*shards, *zones)],
                   jax.ShapeDtypeStruct((8, LANES), F32)),
        in_specs=[HBM] * (2 * ns),
        out_specs=(SEM, SEM, *[HBM] * (2 * ns), pl.BlockSpec(memory_space=pltpu.VMEM)),
        input_output_aliases={i: 2 + i for i in range(2 * ns)},
        compiler_params=pltpu.CompilerParams(has_side_effects=EFFECT),
    )(*[_in_hbm(a) for a in shards], *[_in_hbm(a) for a in zones])
    return out[0], out[1], out[2:2 + ns], out[2 + ns:2 + 2 * ns], out[-1]


def _gather_wait(send_sems, recv_sems, rows, which, shards, zones, after, name):
    ns = len(shards)

    def body(*refs):
        for cp in _first_hop_copies(rows, which, refs[:ns], refs[ns:2 * ns], refs[2 * ns], refs[2 * ns + 1]):
            cp.wait_send()
            cp.wait_recv()

    out = pl.pallas_call(
        body, name=name,
        out_shape=tuple(pltpu.HBM(a.shape, a.dtype) for a in (*shards, *zones)),
        in_specs=[HBM] * (2 * ns) + [SEM, SEM, ANY],
        out_specs=(HBM,) * (2 * ns),
        input_output_aliases={i: i for i in range(2 * ns)},
        compiler_params=pltpu.CompilerParams(has_side_effects=EFFECT),
    )(*shards, *zones, send_sems, recv_sems, after)
    return out[:ns], out[ns:]


def _pass_on_copies(rows, in_refs, out_refs, send_sems, recv_sems):
    ns = len(rows)
    x, y, c = _position()
    return [pltpu.make_async_remote_copy(
        src_ref=_block_of(in_refs[s], 4 * cx + 2 * cy + c, rows[s]),
        dst_ref=_block_of(out_refs[s], 4 * cx + 2 * cy + c, rows[s]),
        send_sem=send_sems.at[j * ns + s], recv_sem=recv_sems.at[j * ns + s],
        device_id=(x, y, 1 - c), device_id_type=MESH_ID)
        for j, (cx, cy) in enumerate(_other_chips(x, y)) for s in range(ns)]


def _pass_on_start(rows, zones, barrier_id, name):
    ns = len(zones)

    def body(*refs):
        _pair_handshake(*_position())
        for cp in _pass_on_copies(rows, refs[:ns], refs[:ns], refs[ns], refs[ns + 1]):
            cp.start()
        refs[-1][...] = jnp.zeros_like(refs[-1])

    out = pl.pallas_call(
        body, name=name,
        out_shape=(pltpu.SemaphoreType.DMA((3 * ns,)), pltpu.SemaphoreType.DMA((3 * ns,)),
                   *[pltpu.HBM(z.shape, z.dtype) for z in zones], jax.ShapeDtypeStruct((8, LANES), F32)),
        in_specs=[HBM] * ns,
        out_specs=(SEM, SEM, *[HBM] * ns, pl.BlockSpec(memory_space=pltpu.VMEM)),
        input_output_aliases={i: 2 + i for i in range(ns)},
        compiler_params=pltpu.CompilerParams(has_side_effects=EFFECT, collective_id=barrier_id),
    )(*[_in_hbm(z) for z in zones])
    return out[0], out[1], out[2:2 + ns], out[-1]


def _pass_on_wait(rows, send_sems, recv_sems, zones, after, name):
    ns = len(zones)

    def body(*refs):
        for cp in _pass_on_copies(rows, refs[:ns], refs[:ns], refs[ns], refs[ns + 1]):
            cp.wait_send()
            cp.wait_recv()

    return pl.pallas_call(
        body, name=name,
        out_shape=tuple(pltpu.HBM(z.shape, z.dtype) for z in zones),
        in_specs=[HBM] * ns + [SEM, SEM, ANY],
        out_specs=(HBM,) * ns,
        input_output_aliases={i: i for i in range(ns)},
        compiler_params=pltpu.CompilerParams(has_side_effects=EFFECT),
    )(*zones, send_sems, recv_sems, after)


def _gather_pass_on(rows, zones, barrier_id, name):
    ns = len(zones)

    def body(*refs):
        _pair_handshake(*_position())
        copies = _pass_on_copies(rows, refs[:ns], refs[ns:2 * ns], *refs[2 * ns:])
        for cp in copies:
            cp.start()
        for cp in copies:
            cp.wait_recv()
        for cp in copies:
            cp.wait_send()

    return pl.pallas_call(
        body, name=name,
        out_shape=tuple(jax.ShapeDtypeStruct(z.shape, z.dtype) for z in zones),
        in_specs=[ANY] * ns, out_specs=(ANY,) * ns,
        input_output_aliases={i: i for i in range(ns)},
        scratch_shapes=[pltpu.SemaphoreType.DMA((3 * ns,)), pltpu.SemaphoreType.DMA((3 * ns,))],
        compiler_params=pltpu.CompilerParams(collective_id=barrier_id),
    )(*zones)


def _pair_copies(sections, g_refs, land, send_sems, recv_sems):
    ns = len(sections)
    x, y, c = _position()
    return [pltpu.make_async_remote_copy(
        src_ref=_block_of(g_refs[s], 2 * k + 1 - c, rows), dst_ref=land[s].at[k],
        send_sem=send_sems.at[k * ns + s], recv_sem=recv_sems.at[k * ns + s],
        device_id=(x, y, 1 - c), device_id_type=MESH_ID)
        for k in range(N_CHIPS) for s, (_, rows, _) in enumerate(sections)]


def _pair_exchange_start(sections, grads, barrier_id, name):
    ns = len(sections)

    def body(*refs):
        _pair_handshake(*_position())
        for cp in _pair_copies(sections, refs[:ns], refs[ns:2 * ns], refs[2 * ns], refs[2 * ns + 1]):
            cp.start()
        refs[-1][...] = jnp.zeros_like(refs[-1])

    zones = [lax.empty((N_CHIPS, rows, cols), BF16) for _, rows, cols in sections]
    n = N_CHIPS * ns
    out = pl.pallas_call(
        body, name=name,
        out_shape=(pltpu.SemaphoreType.DMA((n,)), pltpu.SemaphoreType.DMA((n,)),
                   *[pltpu.HBM(a.shape, a.dtype) for a in (*grads, *zones)],
                   jax.ShapeDtypeStruct((8, LANES), F32)),
        in_specs=[HBM] * (2 * ns),
        out_specs=(SEM, SEM, *[HBM] * (2 * ns), pl.BlockSpec(memory_space=pltpu.VMEM)),
        input_output_aliases={i: 2 + i for i in range(2 * ns)},
        compiler_params=pltpu.CompilerParams(has_side_effects=EFFECT, collective_id=barrier_id),
    )(*[_in_hbm(a) for a in grads], *[_in_hbm(a) for a in zones])
    return out[0], out[1], out[2:2 + ns], out[2 + ns:2 + 2 * ns], out[-1]


def _pair_exchange_wait(sections, send_sems, recv_sems, grads, zones, after, name):
    ns = len(sections)

    def body(*refs):
        for cp in _pair_copies(sections, refs[:ns], refs[ns:2 * ns], refs[2 * ns], refs[2 * ns + 1]):
            cp.wait_send()
            cp.wait_recv()

    out = pl.pallas_call(
        body, name=name,
        out_shape=tuple(pltpu.HBM(a.shape, a.dtype) for a in (*grads, *zones)),
        in_specs=[HBM] * (2 * ns) + [SEM, SEM, ANY],
        out_specs=(HBM,) * (2 * ns),
        input_output_aliases={i: i for i in range(2 * ns)},
        compiler_params=pltpu.CompilerParams(has_side_effects=EFFECT),
    )(*grads, *zones, send_sems, recv_sems, after)
    return out[:ns], out[ns:]


def _pair_add(sections, grads, got, core, name):
    ns = len(sections)

    def body(core_ref, *refs):
        g_refs, got_refs, p_refs = refs[:ns], refs[ns:2 * ns], refs[2 * ns:]
        for s in range(ns):
            p_refs[s][0] = (g_refs[s][...].astype(F32) + got_refs[s][0].astype(F32)).astype(BF16)

    slot = [pl.BlockSpec((1, rows, cols), lambda k, c: (k, 0, 0)) for _, rows, cols in sections]
    return pl.pallas_call(
        body, name=name,
        out_shape=tuple(jax.ShapeDtypeStruct((N_CHIPS, rows, cols), BF16) for _, rows, cols in sections),
        grid_spec=pltpu.PrefetchScalarGridSpec(
            num_scalar_prefetch=1, grid=(N_CHIPS,),
            in_specs=[pl.BlockSpec((rows, cols), lambda k, c: (2 * k + c[0], 0)) for _, rows, cols in sections]
            + slot,
            out_specs=tuple(slot)),
        compiler_params=_params(dimension_semantics=("parallel",)),
    )(core, *grads, *got)


def _chip_copies(sections, p_refs, land, send_sems, recv_sems):
    ns = len(sections)
    x, y, c = _position()
    return [pltpu.make_async_remote_copy(
        src_ref=p_refs[s].at[2 * cx + cy], dst_ref=land[s].at[j],
        send_sem=send_sems.at[j * ns + s], recv_sem=recv_sems.at[j * ns + s],
        device_id=(cx, cy, c), device_id_type=MESH_ID)
        for j, (cx, cy) in enumerate(_other_chips(x, y)) for s in range(ns)]


def _chip_exchange(sections, parts, name):
    ns = len(sections)

    def body(*refs):
        copies = _chip_copies(sections, refs[:ns], refs[ns:2 * ns], *refs[2 * ns:])
        for cp in copies:
            cp.start()
        for cp in copies:
            cp.wait_recv()
        for cp in copies:
            cp.wait_send()

    n = 3 * ns
    return pl.pallas_call(
        body, name=name,
        out_shape=tuple(jax.ShapeDtypeStruct((3, rows, cols), BF16) for _, rows, cols in sections),
        in_specs=[ANY] * ns, out_specs=(ANY,) * ns,
        scratch_shapes=[pltpu.SemaphoreType.DMA((n,)), pltpu.SemaphoreType.DMA((n,))],
    )(*parts)


def _chip_exchange_start(sections, parts, name):
    ns = len(sections)

    def body(*refs):
        p_refs, land = refs[:ns], refs[ns:2 * ns]
        send_sems, recv_sems = refs[2 * ns], refs[2 * ns + 1]
        token = refs[-1]
        for cp in _chip_copies(sections, p_refs, land, send_sems, recv_sems):
            cp.start()
        token[...] = jnp.zeros_like(token)

    zones = [lax.empty((3, rows, cols), BF16) for _, rows, cols in sections]
    out = pl.pallas_call(
        body, name=name,
        out_shape=(pltpu.SemaphoreType.DMA((3 * ns,)), pltpu.SemaphoreType.DMA((3 * ns,)),
                   *[pltpu.HBM(a.shape, a.dtype) for a in parts], *[pltpu.HBM(a.shape, a.dtype) for a in zones],
                   jax.ShapeDtypeStruct((8, LANES), F32)),
        in_specs=[HBM] * (2 * ns),
        out_specs=(SEM, SEM, *[HBM] * (2 * ns), pl.BlockSpec(memory_space=pltpu.VMEM)),
        input_output_aliases={i: 2 + i for i in range(2 * ns)},
        compiler_params=pltpu.CompilerParams(has_side_effects=EFFECT),
    )(*[_in_hbm(a) for a in parts], *[_in_hbm(a) for a in zones])
    return out[0], out[1], out[2:2 + ns], out[2 + ns:2 + 2 * ns], out[-1]


def _chip_exchange_wait(sections, send_sems, recv_sems, parts, zones, after, name):
    ns = len(sections)

    def body(*refs):
        p_refs, land = refs[:ns], refs[ns:2 * ns]
        for cp in _chip_copies(sections, p_refs, land, refs[2 * ns], refs[2 * ns + 1]):
            cp.wait_send()
            cp.wait_recv()

    out = pl.pallas_call(
        body, name=name,
        out_shape=tuple(pltpu.HBM(a.shape, a.dtype) for a in (*parts, *zones)),
        in_specs=[HBM] * (2 * ns) + [SEM, SEM, ANY],
        out_specs=(HBM,) * (2 * ns),
        input_output_aliases={i: i for i in range(2 * ns)},
        compiler_params=pltpu.CompilerParams(has_side_effects=EFFECT),
    )(*parts, *zones, send_sems, recv_sems, after)
    return out[:ns], out[ns:]


def _grad_finish(sections, parts, far, chip, name):
    ns = len(sections)

    def body(chip_ref, *refs):
        p_refs, b_refs, g_refs = refs[:ns], refs[ns:2 * ns], refs[2 * ns:]
        for s in range(ns):
            g = p_refs[s][0].astype(F32)
            for j in range(3):
                g = g + b_refs[s][j].astype(F32)
            g_refs[s][...] = g

    half = [(rows // 2, cols) for _, rows, cols in sections]
    return pl.pallas_call(
        body, name=name,
        out_shape=tuple(jax.ShapeDtypeStruct((rows, cols), F32) for _, rows, cols in sections),
        grid_spec=pltpu.PrefetchScalarGridSpec(
            num_scalar_prefetch=1, grid=(2,),
            in_specs=[pl.BlockSpec((1, r, c), lambda i, chip: (chip[0], i, 0)) for r, c in half]
            + [pl.BlockSpec((3, r, c), lambda i, chip: (0, i, 0)) for r, c in half],
            out_specs=tuple(pl.BlockSpec((r, c), lambda i, chip: (i, 0)) for r, c in half)),
        compiler_params=_params(dimension_semantics=("parallel",)),
    )(chip, *parts, *far)


def _sum_devices(parts, rows, name):
    cols = parts.shape[1]
    tr = rows // 2

    def body(*refs):
        s = refs[0][...].astype(F32)
        for d in range(1, N_DEV):
            s = s + refs[d][...].astype(F32)
        refs[N_DEV][...] = s

    return pl.pallas_call(
        body, name=name,
        out_shape=jax.ShapeDtypeStruct((rows, cols), F32),
        grid=(2,),
        in_specs=[pl.BlockSpec((tr, cols), lambda i, d=d: (2 * d + i, 0)) for d in range(N_DEV)],
        out_specs=pl.BlockSpec((tr, cols), lambda i: (i, 0)),
        compiler_params=_params(dimension_semantics=("parallel",)),
    )(*([parts] * N_DEV))


ADAMW_CHUNK_BYTES = 512 << 10
ADAMW_BUFFERS = 3
ADAMW_BLOCK = 4 * 1024
ADAMW_BLOCK_COLS = 512


def _adamw_step(w_ref, g_ref, m_ref, v_ref, d_ref, nm_ref, nv_ref):
    c1 = 1.0 / (1.0 - ADAM_B1 ** ADAM_STEP)
    c2 = 1.0 / (1.0 - ADAM_B2 ** ADAM_STEP)

    def block(at):
        gv = g_ref[at]
        nm = ADAM_B1 * m_ref[at] + (1.0 - ADAM_B1) * gv
        nv = ADAM_B2 * v_ref[at] + (1.0 - ADAM_B2) * (gv * gv)
        nm_ref[at] = nm
        nv_ref[at] = nv
        d_ref[at] = (-ADAM_LR) * ((nm * c1) / (jnp.sqrt(nv * c2) + ADAM_EPS) + ADAM_WD * w_ref[at])

    rows, cols = w_ref.shape
    bc = min(cols, ADAMW_BLOCK_COLS)
    br = ADAMW_BLOCK // bc
    if rows % br or cols % bc:
        block((slice(None), slice(None)))
        return

    def some_rows(i, carry):
        r = pl.ds(pl.multiple_of(i * br, br), br)
        for c in range(0, cols, bc):
            block((r, pl.ds(c, bc)))
        return carry

    lax.fori_loop(0, rows // br, some_rows, 0)


def _adamw_small(params, name):
    n = len(params)

    def body(*refs):
        for k in range(n):
            _adamw_step(*refs[4 * k:4 * k + 4], *refs[4 * n + 3 * k:4 * n + 3 * k + 3])

    out = pl.pallas_call(
        body, name=name,
        out_shape=tuple(jax.ShapeDtypeStruct(p[0].shape, F32) for p in params for _ in range(3)),
    )(*[a for p in params for a in p])
    return [out[3 * k:3 * k + 3] for k in range(n)]


def _adamw(w, g, m, v, name, after=None):
    rows, cols = w.shape
    ch = max(c for c in range(8, rows + 1, 8)
             if rows % c == 0 and 4 * c <= rows and c * cols * 4 <= ADAMW_CHUNK_BYTES)
    n = rows // ch
    nbuf = min(ADAMW_BUFFERS, n)
    tokens = [] if after is None else [after]

    def body(*refs):
        ins, outs = refs[:4], refs[4 + len(tokens):7 + len(tokens)]
        ibuf, obuf, isem, osem = refs[7 + len(tokens):]

        def read(i, k):
            return pltpu.make_async_copy(ins[k].at[pl.ds(i * ch, ch)], ibuf.at[i % nbuf, k], isem.at[i % nbuf, k])

        def write(i, k):
            return pltpu.make_async_copy(obuf.at[i % nbuf, k], outs[k].at[pl.ds(i * ch, ch)], osem.at[i % nbuf, k])

        for i in range(nbuf):
            for k in range(4):
                read(i, k).start()
        for i in range(n):
            slot = i % nbuf
            for k in range(4):
                read(i, k).wait()
            if i >= nbuf:
                for k in range(3):
                    write(i - nbuf, k).wait()
            _adamw_step(*[ibuf.at[slot, k] for k in range(4)], *[obuf.at[slot, k] for k in range(3)])
            for k in range(3):
                write(i, k).start()
            if i + nbuf < n:
                for k in range(4):
                    read(i + nbuf, k).start()
        for i in range(n - nbuf, n):
            for k in range(3):
                write(i, k).wait()

    shape = jax.ShapeDtypeStruct((rows, cols), F32)
    return pl.pallas_call(
        body, name=name,
        out_shape=(shape, shape, shape),
        in_specs=[ANY] * (4 + len(tokens)), out_specs=(ANY,) * 3,
        scratch_shapes=[pltpu.VMEM((nbuf, 4, ch, cols), F32), pltpu.VMEM((nbuf, 3, ch, cols), F32),
                        pltpu.SemaphoreType.DMA((nbuf, 4)), pltpu.SemaphoreType.DMA((nbuf, 3))],
        compiler_params=_params(),
    )(w, g, m, v, *tokens)


NAMES = ("ln1_g", "w_in", "b_in", "rpb", "w_att_o", "conv_w", "conv_b", "w_rg_a", "b_rg_a", "w_rg_i",
         "b_rg_i", "lru_lambda", "w_rec_o", "w_out", "ln2_g", "w_ff1", "w_ff2", "lnf_g")
TRANSPOSED = {"w_in": "w_in_t", "w_att_o": "w_att_o_t", "w_ff1": "w_ff1_t"}
ROW_SHARDED = ("w_rec_o", "w_out", "w_ff2")
REPLICATED = (("ln1_g", (1, D)), ("b_in", (1, D_IN)), ("rpb", (N_HEADS * N_RPB_R, N_RPB_C)),
              ("conv_b", (1, D_REC)), ("w_rg_a", (2 * N_REC_BLOCKS * REC_BLOCK, REC_BLOCK)),
              ("w_rg_i", (2 * N_REC_BLOCKS * REC_BLOCK, REC_BLOCK)), ("ln2_g", (1, D)), ("lnf_g", (1, D)))
GATE_BLOCKS = ("w_rg_a", "w_rg_i")
SMALL_ROWS = 112


def _chan_bits(vectors):
    chan = jnp.concatenate(vectors, axis=0)
    bits = lax.bitcast_convert_type(chan, BF16).reshape(-1)
    return jnp.pad(bits, (0, CHAN_BLOCK_ROWS * D - bits.shape[0])).reshape(CHAN_BLOCK_ROWS, D)


def _chan_from_bits(gathered):
    bits = gathered.reshape(N_DEV, CHAN_BLOCK_ROWS * D)[:, :2 * N_CHAN_ROWS * LANES]
    chan = lax.bitcast_convert_type(bits.reshape(N_DEV, N_CHAN_ROWS, LANES, 2), F32)
    return chan.transpose(1, 0, 2).reshape(N_CHAN_ROWS, D)


def kernel(x, ln1_g, w_in, b_in, rpb, w_att_o, conv_w, conv_b, w_rg_a, b_rg_a, w_rg_i, b_rg_i, lru_lambda, w_rec_o, w_out, ln2_g, w_ff1, w_ff2, lnf_g, loss_target, m_ln1_g, m_w_in, m_b_in, m_rpb, m_w_att_o, m_conv_w, m_conv_b, m_w_rg_a, m_b_rg_a, m_w_rg_i, m_b_rg_i, m_lru_lambda, m_w_rec_o, m_w_out, m_ln2_g, m_w_ff1, m_w_ff2, m_lnf_g, v_ln1_g, v_w_in, v_b_in, v_rpb, v_w_att_o, v_conv_w, v_conv_b, v_w_rg_a, v_b_rg_a, v_w_rg_i, v_b_rg_i, v_lru_lambda, v_w_rec_o, v_w_out, v_ln2_g, v_w_ff1, v_w_ff2, v_lnf_g):
    w = dict(zip(NAMES, (ln1_g, w_in, b_in, rpb, w_att_o, conv_w, conv_b, w_rg_a, b_rg_a, w_rg_i,
                         b_rg_i, lru_lambda, w_rec_o, w_out, ln2_g, w_ff1, w_ff2, lnf_g)))
    m = dict(zip(NAMES, (m_ln1_g, m_w_in, m_b_in, m_rpb, m_w_att_o, m_conv_w, m_conv_b, m_w_rg_a,
                         m_b_rg_a, m_w_rg_i, m_b_rg_i, m_lru_lambda, m_w_rec_o, m_w_out, m_ln2_g,
                         m_w_ff1, m_w_ff2, m_lnf_g)))
    v = dict(zip(NAMES, (v_ln1_g, v_w_in, v_b_in, v_rpb, v_w_att_o, v_conv_w, v_conv_b, v_w_rg_a,
                         v_b_rg_a, v_w_rg_i, v_b_rg_i, v_lru_lambda, v_w_rec_o, v_w_out, v_ln2_g,
                         v_w_ff1, v_w_ff2, v_lnf_g)))
    xi, yi, ci = _position()

    shard = {t: w[n][0].T.astype(BF16) for n, t in TRANSPOSED.items()}
    shard.update({n: w[n][0].astype(BF16) for n in ROW_SHARDED})
    shard["chan"] = _chan_bits([w[n][0] for n, _ in CHAN])
    first, later = ("w_in_t", "chan"), ("w_rec_o", "w_out", "w_att_o_t", "w_ff1_t", "w_ff2")
    p = dict(zip(first, _all_gather([shard[n] for n in first], "weight_all_gather", (4, 1))))
    send_sems, recv_sems, sent, zones, token = _gather_start([shard[n] for n in later], p["w_in_t"],
                                                             "weight_gather_start")

    stages = (("w_rec_o", "w_out", "w_att_o_t"), ("w_ff1_t", "w_ff2"))
    passing = {}

    def late_weights(after, stage):
        which = [later.index(n) for n in stages[stage]]
        _, arrived = _gather_wait(
            send_sems, recv_sems, [shard[n].shape[0] for n in later], which, [sent[i] for i in which],
            [zones[i] for i in which], after, "weight_gather_wait_%d" % stage)
        passing[stage] = _pass_on_start(
            [shard[n].shape[0] for n in stages[stage]], arrived,
            PASS_ON_IDS[stage], "weight_pass_on_start_%d" % stage)
        return passing[stage][-1]

    def late_weights_ready(after, stage):
        pass_send_sems, pass_recv_sems, pass_zones, _ = passing[stage]
        return dict(zip(stages[stage], _pass_on_wait(
            [shard[n].shape[0] for n in stages[stage]], pass_send_sems, pass_recv_sems, pass_zones, after,
            "weight_pass_on_wait_%d" % stage)))

    chan = _chan_from_bits(p.pop("chan"))
    r0 = 0
    for n, rows in CHAN:
        p[n] = chan[r0:r0 + rows]
        r0 += rows
    p.update(ln1_g=w["ln1_g"], b_in=w["b_in"], later_weights_started=token, rpb=w["rpb"][0], conv_b=w["conv_b"],
             w_rg_a=w["w_rg_a"][0], w_rg_i=w["w_rg_i"][0], ln2_g=w["ln2_g"],
             lnf_g=w["lnf_g"].reshape(1, D))

    core = jnp.reshape(ci, (1,)).astype(jnp.int32)
    chip = jnp.reshape(2 * xi + yi, (1,)).astype(jnp.int32)
    first_sections = tuple(s for s in SECTIONS if s[0] in ("w_ff1_t", "w_ff2"))
    late_sections = SECTIONS[:1]
    early_sections = tuple(s for s in SECTIONS[1:] if s not in first_sections)
    in_flight = {}

    def pair_sum_and_send(group, sections, after):
        send_sems, recv_sems, sect, zones, _ = in_flight["pair_" + group]
        sect, got = _pair_exchange_wait(sections, send_sems, recv_sems, sect, zones, after,
                                        "grad_pair_exchange_wait_" + group)
        parts = _pair_add(sections, sect, got, core, "grad_pair_add_" + group)
        in_flight[group] = _chip_exchange_start(sections, parts, "grad_chip_exchange_start_" + group)
        return in_flight[group][-1]

    def pair_exchange_at_once(group, sections, grads, barrier_id):
        in_flight["pair_" + group] = _pair_exchange_start(
            sections, [grads[n] for n, _, _ in sections], barrier_id, "grad_pair_exchange_start_" + group)
        return pair_sum_and_send(group, sections, in_flight["pair_" + group][-1])

    def reduce_first(grads, after):
        if grads is None:
            return pair_sum_and_send("first", first_sections, after)
        in_flight["pair_first"] = _pair_exchange_start(
            first_sections, [grads[n] for n, _, _ in first_sections], PAIR_FIRST_ID,
            "grad_pair_exchange_start_first")
        return in_flight["pair_first"][-1]

    def reduce_early(grads):
        chan_g = jnp.concatenate([grads[n] for n, _ in CHAN], axis=0)
        chan_g = chan_g.reshape(N_CHAN_ROWS, N_DEV, LANES).transpose(1, 0, 2).astype(BF16)
        chan_g = jnp.pad(chan_g.reshape(N_DEV, -1), ((0, 0), (0, CHAN_BLOCK_ROWS * D - N_CHAN_ROWS * LANES)))
        grads["chan"] = chan_g.reshape(N_DEV * CHAN_BLOCK_ROWS, D)
        grads["gates"] = jnp.concatenate([grads[n].reshape(-1, D) for n in GATE_BLOCKS], axis=0).astype(BF16)
        return pair_exchange_at_once("early", early_sections, grads, PAIR_EARLY_ID)

    def finish(group, sections, after, name):
        send_sems, recv_sems, parts, zones, _ = in_flight[group]
        parts, far = _chip_exchange_wait(sections, send_sems, recv_sems, parts, zones, after,
                                         "grad_chip_exchange_wait_" + name)
        return dict(zip((n for n, _, _ in sections),
                        _grad_finish(sections, parts, far, chip, "grad_finish_" + name)))

    summed = {}

    def reduce_late(grads):
        in_flight["pair_late"] = _pair_exchange_start(
            late_sections, [grads[n] for n, _, _ in late_sections], PAIR_LATE_ID,
            "grad_pair_exchange_start_late")
        summed.update(finish("first", first_sections, in_flight["pair_late"][-1], "first"))
        summed.update(finish("early", early_sections, summed["w_ff2"], "early"))
        return pair_sum_and_send("late", late_sections, summed["gates"])

    loss_part, grad_x, grads = _local_step(x[0], loss_target[0], p, late_weights, late_weights_ready,
                                           reduce_first, reduce_early, reduce_late)

    flat = jnp.concatenate([grads[n].reshape(-1) for n, _ in REPLICATED if n not in GATE_BLOCKS]
                           + [loss_part.reshape(-1)])
    n_small = flat.shape[0]
    flat = jnp.pad(flat, (0, SMALL_ROWS * LANES - n_small)).reshape(SMALL_ROWS, LANES)
    *small_gather, small_started = _gather_start([flat, summed["gates"]], None, "small_grad_gather_start")

    g, delta, new_m, new_v = {}, {}, {}, {}

    def update(n, g2, shape2, after=None):
        d2, m2, v2 = _adamw(w[n].reshape(shape2), g2, m[n].reshape(shape2), v[n].reshape(shape2),
                            "adamw_" + n, after)
        g[n], delta[n], new_m[n], new_v[n] = (a.reshape(w[n].shape) for a in (g2, d2, m2, v2))

    for n in ROW_SHARDED:
        update(n, summed[n], summed[n].shape, small_started)
    for n, t in TRANSPOSED.items():
        if t in summed:
            update(n, summed[t].T, summed[t].shape[::-1], small_started)

    small_rows = [SMALL_ROWS, GATE_ROWS]
    _, small_zones = _gather_wait(
        *small_gather[:2], small_rows, range(2), *small_gather[2:],
        _after_all(list(delta.values()), "sharded_updates_done"), "small_grad_gather_wait")
    small_parts, gate_sum = _gather_pass_on(small_rows, small_zones, SMALL_PASS_ON_ID,
                                            "small_grad_gather_pass_on")
    small = _sum_devices(small_parts, SMALL_ROWS, "small_grad_sum").reshape(-1)
    loss = small[n_small - 1]

    small_params = []
    o = 0
    for n, shape2 in REPLICATED:
        if n in GATE_BLOCKS:
            k, rows = GATE_BLOCKS.index(n), gate_sum.shape[0] // len(GATE_BLOCKS)
            update(n, gate_sum[k * rows:(k + 1) * rows].reshape(shape2), shape2)
        else:
            size = shape2[0] * shape2[1]
            small_params.append((n, small[o:o + size].reshape(shape2), shape2))
            o += size
    chan_back = summed["chan"].reshape(-1)[:N_CHAN_ROWS * LANES].reshape(N_CHAN_ROWS, LANES)
    r0 = 0
    for n, rows in CHAN:
        small_params.append((n, chan_back[r0:r0 + rows], (rows, LANES)))
        r0 += rows
    results = _adamw_small([(w[n].reshape(s2), g2, m[n].reshape(s2), v[n].reshape(s2))
                            for n, g2, s2 in small_params], "adamw_vectors")
    for (n, g2, _), (d2, m2, v2) in zip(small_params, results):
        g[n], delta[n], new_m[n], new_v[n] = (a.reshape(w[n].shape) for a in (g2, d2, m2, v2))

    summed = finish("late", late_sections, _after_all(list(delta.values()), "updates_done"), "late")
    g_t = summed["w_in_t"]
    results = _adamw(w["w_in"][0].T, g_t, m["w_in"][0].T, v["w_in"][0].T, "adamw_w_in")
    g["w_in"], delta["w_in"], new_m["w_in"], new_v["w_in"] = (a.T[None] for a in (g_t, *results))

    return (loss, grad_x[None], *[g[n] for n in NAMES], *[delta[n] for n in NAMES],
            *[new_m[n] for n in NAMES], *[new_v[n] for n in NAMES])
```

```python
import math

import numpy as np
import jax
import jax.numpy as jnp
from jax import lax
from jax.experimental import pallas as pl
from jax.experimental.pallas import tpu as pltpu

F32 = jnp.float32
BF16 = jnp.bfloat16

T = 2048
D = 1024
D_ATT = 512
D_REC = 1024
D_FF = 4096
D_IN = 5632
N_HEADS = 8
DH = 64
GRID_W = 64
ROWS = T // GRID_W
WIN_H = 8
WIN_W = 16
KWIN = WIN_H * GRID_W
N_RPB_R = 2 * WIN_H - 1
N_RPB_C = 2 * WIN_W - 1
N_REC_BLOCKS = 16
REC_BLOCK = 64
CG = 128
N_CG = D_REC // CG
LRU_C = 8.0
EPS = 1e-6
N_DEV = 8
N_CHIPS = 4
LANES = 128

ADAM_LR = 0.001
ADAM_B1 = 0.9
ADAM_B2 = 0.999
ADAM_EPS = 1e-08
ADAM_WD = 0.01
ADAM_STEP = 10

MESH_AXES = ("x", "y", "c")
VMEM_LIMIT = 56 * 1024 * 1024

TILE = 512
DZ_ARRAYS = ((0, 3, 1), (3, 4, 2), (7, 4, 2))
N_DZ_TILES = D_IN // TILE


def _params(**kw):
    return pltpu.CompilerParams(vmem_limit_bytes=VMEM_LIMIT, **kw)


HG = 4
HQ = HG * GRID_W
HC = HG * DH


def _att_tables():
    rq = np.arange(GRID_W)
    kc = np.arange(KWIN) % GRID_W
    win_start = np.clip(rq - WIN_W // 2, 0, GRID_W - WIN_W)
    valid = (kc[None, :] >= win_start[:, None]) & (kc[None, :] < win_start[:, None] + WIN_W)
    same_head = (np.arange(HQ)[:, None] // GRID_W) == (np.arange(HC)[None, :] // DH)
    return valid.astype(np.float32), same_head.astype(np.float32)


def _pair_mask():
    half = np.arange(2 * DH) // DH
    return (half[:, None] == half[None, :]).astype(np.float32)


def _dup_table():
    return np.concatenate([np.eye(REC_BLOCK, dtype=np.float32)] * 2, axis=1)


def _sigmoid(x):
    return 0.5 * jnp.tanh(0.5 * x) + 0.5


def _softplus(x):
    return jnp.maximum(x, 0.0) + jnp.log(1.0 + jnp.exp(-jnp.abs(x)))


def _one_minus_square(log_a, a):
    x = 2.0 * log_a
    series = -x * (1.0 + x * (0.5 + x * (1.0 / 6.0)))
    return jnp.where(x > -0.02, series, 1.0 - a * a)


_GELU_C = math.sqrt(2.0 / math.pi)


def _gelu_and_grad(x):
    x2 = x * x
    inner = _GELU_C * (x + 0.044715 * x * x2)
    t = jnp.tanh(inner)
    g = 0.5 * x * (1.0 + t)
    dg = 0.5 * (1.0 + t) + 0.5 * x * (1.0 - t * t) * _GELU_C * (1.0 + 3.0 * 0.044715 * x2)
    return g, dg


def _dot(a, b):
    return jnp.dot(a, b, preferred_element_type=F32)


def _dot_nt(a, b):
    return lax.dot_general(a, b, (((1,), (1,)), ((), ())), preferred_element_type=F32)


def _dot_tn(a, b):
    return lax.dot_general(a, b, (((0,), (0,)), ((), ())), preferred_element_type=F32)


def _dot_exact(a, b):
    return jnp.dot(a, b, precision=lax.Precision.HIGHEST, preferred_element_type=F32)


def _select_rows(onehot, x):
    sel = onehot.astype(BF16)
    out = None
    for _ in range(3):
        part = x.astype(BF16)
        x = x - part.astype(F32)
        out = _dot(sel, part) if out is None else out + _dot(sel, part)
    return out


def _shift_rows(x, s):
    n = x.shape[0]
    rows = lax.broadcasted_iota(jnp.int32, x.shape, 0)
    y = pltpu.roll(x, s % n, 0)
    if s > 0:
        return jnp.where(rows >= s, y, 0.0)
    return jnp.where(rows < n + s, y, 0.0)


def _rms_bwd(dh, xh, r, g):
    dxh = dh * g
    return r * (dxh - xh * jnp.mean(dxh * xh, axis=-1, keepdims=True))


def _matmul(a, b, mode, out_dtype, name, tm=512, tn=1024, tk=2048):
    if mode == "nn":
        (m, k), (k2, n) = a.shape, b.shape
    elif mode == "nt":
        (m, k), (n, k2) = a.shape, b.shape
    else:
        (k, m), (k2, n) = a.shape, b.shape
    assert k == k2
    tm, tn, tk = min(tm, m), min(tn, n), min(tk, k)
    assert m % tm == 0 and n % tn == 0 and k % tk == 0
    nk = k // tk
    dot = {"nn": _dot, "nt": _dot_nt, "tn": _dot_tn}[mode]

    def body(a_ref, b_ref, o_ref, acc):
        kk = pl.program_id(2)
        part = dot(a_ref[...].astype(BF16), b_ref[...].astype(BF16))
        if nk == 1:
            o_ref[...] = part.astype(out_dtype)
            return

        @pl.when(kk == 0)
        def _():
            acc[...] = part

        @pl.when(kk > 0)
        def _():
            acc[...] += part

        @pl.when(kk == nk - 1)
        def _():
            o_ref[...] = acc[...].astype(out_dtype)

    if mode == "tn":
        a_spec = pl.BlockSpec((tk, tm), lambda i, j, kk: (kk, i))
    else:
        a_spec = pl.BlockSpec((tm, tk), lambda i, j, kk: (i, kk))
    if mode == "nt":
        b_spec = pl.BlockSpec((tn, tk), lambda i, j, kk: (j, kk))
    else:
        b_spec = pl.BlockSpec((tk, tn), lambda i, j, kk: (kk, j))
    return pl.pallas_call(
        body, name=name,
        out_shape=jax.ShapeDtypeStruct((m, n), out_dtype),
        grid=(m // tm, n // tn, nk),
        in_specs=[a_spec, b_spec],
        out_specs=pl.BlockSpec((tm, tn), lambda i, j, kk: (i, j)),
        scratch_shapes=[pltpu.VMEM((tm, tn) if nk > 1 else (8, LANES), F32)],
        compiler_params=_params(dimension_semantics=("parallel", "parallel", "arbitrary")),
    )(a, b)


def _in_proj(x, g1, w_in_t, b_in, after):
    tm = 512

    def body(x_ref, g_ref, w_hbm, b_ref, after_ref, qkv_ref, uy_ref, gg_ref, h_ref, w):
        @pl.when(pl.program_id(0) == 0)
        def _():
            pltpu.sync_copy(w_hbm, w)

        xv = x_ref[...]
        r = lax.rsqrt(jnp.mean(xv * xv, axis=-1, keepdims=True) + EPS)
        h = ((xv * r) * g_ref[...]).astype(BF16)
        h_ref[...] = h
        row0 = 0
        for ref in (qkv_ref, uy_ref, gg_ref):
            for c0 in range(0, ref.shape[1], TILE):
                z = _dot_nt(h, w[row0:row0 + TILE, :]) + b_ref[:, row0:row0 + TILE]
                ref[:, c0:c0 + TILE] = z.astype(ref.dtype)
                row0 += TILE

    tok = lambda width: pl.BlockSpec((tm, width), lambda i: (i, 0))
    return pl.pallas_call(
        body, name="in_proj",
        out_shape=(jax.ShapeDtypeStruct((T, 3 * D_ATT), BF16),
                   jax.ShapeDtypeStruct((T, 2 * D_REC), F32),
                   jax.ShapeDtypeStruct((T, 2 * D), F32),
                   jax.ShapeDtypeStruct((T, D), BF16)),
        grid=(T // tm,),
        in_specs=[tok(D), pl.BlockSpec((1, D), lambda i: (0, 0)), pl.BlockSpec(memory_space=pl.ANY),
                  pl.BlockSpec((1, D_IN), lambda i: (0, 0)), pl.BlockSpec(memory_space=pl.ANY)],
        out_specs=(tok(3 * D_ATT), tok(2 * D_REC), tok(2 * D), tok(D)),
        scratch_shapes=[pltpu.VMEM((D_IN, D), BF16)],
        compiler_params=_params(dimension_semantics=("arbitrary",)),
    )(x, g1, w_in_t, b_in, after)


def _dz_specs(rows, tile_of, row_of):
    def spec(off, n, per_plane):
        def index(*ids):
            t = jnp.clip(tile_of(*ids) - off, 0, n - 1)
            return (t // per_plane, row_of(*ids), t % per_plane)
        return pl.BlockSpec((1, rows, TILE), index)
    return [spec(off, n, per) for off, n, per in DZ_ARRAYS]


def _dh_norm1_bwd(dz, w_in_t, x, g1, dx1, after):
    tm = 512

    def body(dqkv_ref, duy_ref, dgg_ref, w_hbm, x_ref, g_ref, dx1_ref, after_ref, gx_ref, dg_ref, w):
        @pl.when(pl.program_id(0) == 0)
        def _():
            pltpu.sync_copy(w_hbm, w)
            dg_ref[...] = jnp.zeros_like(dg_ref)

        dh, row0 = None, 0
        for ref in (dqkv_ref, duy_ref, dgg_ref):
            for plane in range(ref.shape[0]):
                cols = ref.shape[2]
                part = _dot(ref[plane], w[row0:row0 + cols, :])
                dh = part if dh is None else dh + part
                row0 += cols
        xv = x_ref[...]
        r = lax.rsqrt(jnp.mean(xv * xv, axis=-1, keepdims=True) + EPS)
        xh = xv * r
        dg_ref[...] += jnp.sum(dh * xh, axis=0, keepdims=True)
        gx_ref[...] = dx1_ref[...] + _rms_bwd(dh, xh, r, g_ref[...])

    tok = pl.BlockSpec((tm, D), lambda i: (i, 0))
    vec = pl.BlockSpec((1, D), lambda i: (0, 0))
    planes = lambda a: pl.BlockSpec((a.shape[0], tm, a.shape[2]), lambda i: (0, i, 0))
    return pl.pallas_call(
        body, name="dh_norm1_bwd",
        out_shape=(jax.ShapeDtypeStruct((T, D), F32), jax.ShapeDtypeStruct((1, D), F32)),
        grid=(T // tm,),
        in_specs=[planes(a) for a in dz] + [pl.BlockSpec(memory_space=pl.ANY), tok, vec, tok,
                                            pl.BlockSpec(memory_space=pl.ANY)],
        out_specs=(tok, vec),
        scratch_shapes=[pltpu.VMEM((D_IN, D), BF16)],
        compiler_params=_params(dimension_semantics=("arbitrary",)),
    )(*dz, w_in_t, x, g1, dx1, after)


def _grad_w_in(dz, h):
    def body(*refs):
        seg_refs = refs[:3]
        h_ref, gw_ref, gb_ref = refs[3:]
        j = pl.program_id(0)

        for s, (off, n, _) in enumerate(DZ_ARRAYS):
            @pl.when((j >= off) & (j < off + n))
            def _(s=s):
                a = seg_refs[s][0]
                gw_ref[...] = _dot_tn(a, h_ref[...]).astype(BF16)
                gb_ref[...] = jnp.sum(a.astype(F32), axis=0, keepdims=True)

    return pl.pallas_call(
        body, name="grad_w_in",
        out_shape=(jax.ShapeDtypeStruct((D_IN, D), BF16), jax.ShapeDtypeStruct((1, D_IN), F32)),
        grid=(N_DZ_TILES,),
        in_specs=_dz_specs(T, lambda j: j, lambda j: 0) + [pl.BlockSpec((T, D), lambda j: (0, 0))],
        out_specs=(pl.BlockSpec((TILE, D), lambda j: (j, 0)), pl.BlockSpec((1, TILE), lambda j: (0, j))),
        compiler_params=_params(dimension_semantics=("parallel",)),
    )(*dz, h)


def _rpb_rows(rpb):
    padded = jnp.pad(rpb, ((0, 0), (0, 0), (0, GRID_W - N_RPB_C)))
    rows = [padded[:, WIN_H - 1 - oi: 2 * WIN_H - 1 - oi].reshape(N_HEADS // HG, HG, KWIN)
            for oi in range(WIN_H)]
    return jnp.stack(rows, axis=0)


SKEW = KWIN - (WIN_W - 1)


MASKED = -1e30


def _bias_tiles(rows_ref, valid, bias_s):
    for oi in range(WIN_H):
        for hh in range(HG):
            row = jnp.broadcast_to(rows_ref[oi, 0, hh:hh + 1, :], (GRID_W, KWIN))
            tile = pltpu.roll(row, SKEW, 1, stride=1, stride_axis=0)
            bias_s[oi, hh * GRID_W:(hh + 1) * GRID_W, :] = jnp.where(valid, tile, MASKED)


def _bias_tile_grads(gb_s, flip, out_ref):
    for oi in range(WIN_H):
        for hh in range(HG):
            g = _select_rows(flip, gb_s[oi, hh * GRID_W:(hh + 1) * GRID_W, :])
            back = pltpu.roll(g, KWIN - (GRID_W - WIN_W), 1, stride=1, stride_axis=0)
            out_ref[0, oi, hh:hh + 1, :] = jnp.sum(back, axis=0, keepdims=True)


def _rpb_fold(row_grads):
    g = row_grads.transpose(1, 0, 2, 3).reshape(WIN_H, N_HEADS, WIN_H, GRID_W)
    g = g.transpose(0, 2, 1, 3)

    def body(g_ref, o_ref):
        for dr in range(N_RPB_R):
            terms = [g_ref[oi, i] for oi in range(WIN_H) for i in range(WIN_H) if i - oi + WIN_H - 1 == dr]
            acc = terms[0]
            for term in terms[1:]:
                acc = acc + term
            o_ref[dr] = acc

    out = pl.pallas_call(
        body, name="rpb_fold",
        out_shape=jax.ShapeDtypeStruct((N_RPB_R, N_HEADS, GRID_W), F32),
    )(g)
    return out.transpose(1, 0, 2)[:, :, :N_RPB_C]


ATT_GROUPS = N_HEADS // HG
ATT_UNROLL = 8


def _stacked(rows64, same_head):
    return jnp.where(same_head, jnp.concatenate([rows64] * HG, axis=0), jnp.zeros((), BF16))


def _own_heads(stacked):
    head = lax.broadcasted_iota(jnp.int32, (GRID_W, HC), 1) // DH
    out = stacked[:GRID_W]
    for h in range(1, HG):
        out = jnp.where(head == h, stacked[h * GRID_W:(h + 1) * GRID_W], out)
    return out


def _att_scores(q_ref, k_ref, bias_ref, same_head, r):
    rs = jnp.clip(r - WIN_H // 2, 0, ROWS - WIN_H)
    oi = r - rs
    q0 = pl.multiple_of(r * GRID_W, GRID_W)
    k0 = pl.multiple_of(rs * GRID_W, GRID_W)
    q2 = _stacked(q_ref[pl.ds(q0, GRID_W), :] * (DH ** -0.5), same_head)
    kw = k_ref[pl.ds(k0, KWIN), :]
    s = _dot_nt(q2, kw) + bias_ref[oi]
    e = jnp.exp(s - jnp.max(s, axis=-1, keepdims=True))
    return e, 1.0 / jnp.sum(e, axis=-1, keepdims=True), q2, kw, q0, k0, oi


def _att_specs():
    col = lambda off: pl.BlockSpec((T, HC), lambda g: (0, g + off * ATT_GROUPS))
    tables = [pl.BlockSpec((WIN_H, 1, HG, KWIN), lambda g: (0, g, 0, 0)),
              pl.BlockSpec((GRID_W, KWIN), lambda g: (0, 0)),
              pl.BlockSpec((HQ, HC), lambda g: (0, 0))]
    return col, tables, pltpu.VMEM((WIN_H, HQ, KWIN), F32)


def _att_fwd(qkv, bias_rows, after):
    valid_np, same_head_np = _att_tables()

    def body(q_ref, k_ref, v_ref, rows_ref, valid_ref, head_ref, after_ref, o_ref, bias_s):
        same_head = head_ref[...] > 0.5
        _bias_tiles(rows_ref, valid_ref[...] > 0.5, bias_s)

        def row(r, carry):
            e, rl, _, _, q0, k0, _ = _att_scores(q_ref, k_ref, bias_s, same_head, r)
            o2 = _dot((e * rl).astype(BF16), v_ref[pl.ds(k0, KWIN), :])
            o_ref[pl.ds(q0, GRID_W), :] = _own_heads(o2).astype(BF16)
            return carry

        lax.fori_loop(0, ROWS, row, 0, unroll=ATT_UNROLL)

    col, tables, tiles = _att_specs()
    return pl.pallas_call(
        body, name="att_fwd",
        out_shape=jax.ShapeDtypeStruct((T, D_ATT), BF16),
        grid=(ATT_GROUPS,),
        in_specs=[col(0), col(1), col(2)] + tables + [pl.BlockSpec(memory_space=pl.ANY)],
        out_specs=col(0),
        scratch_shapes=[tiles],
        compiler_params=_params(dimension_semantics=("parallel",)),
    )(qkv, qkv, qkv, bias_rows, jnp.asarray(valid_np), jnp.asarray(same_head_np), after)


def _att_bwd(qkv, bias_rows, datt, after):
    valid_np, same_head_np = _att_tables()

    def body(q_ref, k_ref, v_ref, do_ref, rows_ref, valid_ref, head_ref, flip_ref, after_ref,
             dqkv_ref, grows_ref, dk_acc, dv_acc, bias_s, gb_s):
        same_head = head_ref[...] > 0.5
        dk_acc[...] = jnp.zeros_like(dk_acc)
        dv_acc[...] = jnp.zeros_like(dv_acc)
        gb_s[...] = jnp.zeros_like(gb_s)
        _bias_tiles(rows_ref, valid_ref[...] > 0.5, bias_s)

        def row(r, carry):
            e, rl, q2, kw, q0, k0, oi = _att_scores(q_ref, k_ref, bias_s, same_head, r)
            do2 = _stacked(do_ref[pl.ds(q0, GRID_W), :], same_head)
            vw = v_ref[pl.ds(k0, KWIN), :]
            p = e * rl
            dp = _dot_nt(do2, vw)
            ds = p * (dp - jnp.sum(dp * p, axis=-1, keepdims=True))
            p16 = p.astype(BF16)
            ds16 = ds.astype(BF16)
            dv_acc[pl.ds(k0, KWIN), :] += _dot_tn(p16, do2)
            dk_acc[pl.ds(k0, KWIN), :] += _dot_tn(ds16, q2)
            dq2 = _dot(ds16, kw) * (DH ** -0.5)
            dqkv_ref[0, pl.ds(q0, GRID_W), :] = _own_heads(dq2).astype(BF16)
            gb_s[oi] += ds
            return carry

        lax.fori_loop(0, ROWS, row, 0, unroll=ATT_UNROLL)
        dqkv_ref[1] = dk_acc[...].astype(BF16)
        dqkv_ref[2] = dv_acc[...].astype(BF16)
        _bias_tile_grads(gb_s, flip_ref[...], grows_ref)

    col, tables, tiles = _att_specs()
    return pl.pallas_call(
        body, name="att_bwd",
        out_shape=(jax.ShapeDtypeStruct((3, T, D_ATT), BF16),
                   jax.ShapeDtypeStruct((ATT_GROUPS, WIN_H, HG, KWIN), F32)),
        grid=(ATT_GROUPS,),
        in_specs=[col(0), col(1), col(2), col(0)] + tables + [pl.BlockSpec((GRID_W, GRID_W), lambda g: (0, 0)),
                                                              pl.BlockSpec(memory_space=pl.ANY)],
        out_specs=(pl.BlockSpec((3, T, HC), lambda g: (0, 0, g)),
                   pl.BlockSpec((1, WIN_H, HG, KWIN), lambda g: (g, 0, 0, 0))),
        scratch_shapes=[pltpu.VMEM((T, HC), F32), pltpu.VMEM((T, HC), F32), tiles, tiles],
        compiler_params=_params(dimension_semantics=("parallel",)),
    )(qkv, qkv, qkv, datt, bias_rows, jnp.asarray(valid_np), jnp.asarray(same_head_np),
      jnp.asarray(np.eye(GRID_W, dtype=np.float32)[::-1].copy()), after)


def _conv_taps(up):
    return (_shift_rows(up, 2), _shift_rows(up, 1), up, _shift_rows(up, -1))


def _pair_block_diag(w_pair, dup, same_half):
    return jnp.where(same_half, _dot(w_pair.astype(BF16), dup), 0.0).astype(BF16)


def _gates(u, u16, wa, ba, wi, bi, lam):
    r = _sigmoid(_dot(u16, wa) + ba)
    ig = _sigmoid(_dot(u16, wi) + bi)
    sp = _softplus(-lam)
    log_a = (-LRU_C) * r * sp
    a = jnp.exp(log_a)
    mult2 = jnp.maximum(_one_minus_square(log_a, a), 0.0)
    return r, ig, sp, a, jnp.sqrt(mult2), mult2


SCAN_BLOCKS = 8


def _scans(jobs):
    c = jobs[0][0].shape[1]
    nblk = T // 8
    rows = lax.broadcasted_iota(jnp.int32, (8, c), 0)

    def block(a, b, reverse):
        for s in (1, 2, 4):
            if reverse:
                keep = rows < 8 - s
                a_s = jnp.where(keep, pltpu.roll(a, 8 - s, 0), 1.0)
                b_s = jnp.where(keep, pltpu.roll(b, 8 - s, 0), 0.0)
            else:
                keep = rows >= s
                a_s = jnp.where(keep, pltpu.roll(a, s, 0), 1.0)
                b_s = jnp.where(keep, pltpu.roll(b, s, 0), 0.0)
            b = a * b_s + b
            a = a * a_s
        return a, b

    def step(i, carry):
        out = []
        for (a_ref, b_ref, h_ref, reverse), h_prev in zip(jobs, carry):
            for u in range(SCAN_BLOCKS):
                blk = i * SCAN_BLOCKS + u
                if reverse:
                    blk = nblk - 1 - blk
                t0 = pl.multiple_of(blk * 8, 8)
                a, b = block(a_ref[pl.ds(t0, 8), :], b_ref[pl.ds(t0, 8), :], reverse)
                h = a * h_prev + b
                h_ref[pl.ds(t0, 8), :] = h
                h_prev = jnp.broadcast_to(h[0:1] if reverse else h[7:8], (8, c))
            out.append(h_prev)
        return tuple(out)

    lax.fori_loop(0, nblk // SCAN_BLOCKS, step, tuple(jnp.zeros((8, c), F32) for _ in jobs))


def _rec_specs():
    tok = lambda off: pl.BlockSpec((T, CG), lambda g: (0, g + off))
    per_ch = lambda rows: pl.BlockSpec((rows, CG), lambda g: (0, g))
    wspec = pl.BlockSpec((2, 1, CG, REC_BLOCK), lambda g: (0, g, 0, 0))
    const = lambda shape: pl.BlockSpec(shape, lambda g: (0, 0))
    return tok, per_ch, wspec, const


def _rec_fwd(uy, conv_w, conv_b, w_a, b_a, w_i, b_i, lam):
    tok, per_ch, wspec, const = _rec_specs()

    def body(up_ref, yb_ref, cw_ref, cb_ref, wa_ref, ba_ref, wi_ref, bi_ref, lam_ref, dup_ref, half_ref,
             hf_ref, hb_ref, yrec_ref, am_ref, bx_f, bx_b):
        dup = dup_ref[...]
        same_half = half_ref[...] > 0.5
        taps = _conv_taps(up_ref[...])
        u = cb_ref[...]
        for j in range(4):
            u = u + taps[j] * cw_ref[j:j + 1, :]
        u16 = u.astype(BF16)
        for d, bx_s in enumerate((bx_f, bx_b)):
            wa = _pair_block_diag(wa_ref[d, 0], dup, same_half)
            wi = _pair_block_diag(wi_ref[d, 0], dup, same_half)
            _, ig, _, a, mult, _ = _gates(u, u16, wa, ba_ref[d:d + 1, :], wi, bi_ref[d:d + 1, :],
                                       lam_ref[d:d + 1, :])
            am_ref[2 * d] = a
            am_ref[2 * d + 1] = mult
            bx_s[...] = mult * (ig * u)
        _scans([(am_ref.at[0], bx_f, hf_ref, False), (am_ref.at[2], bx_b, hb_ref, True)])
        gelu, _ = _gelu_and_grad(yb_ref[...])
        yrec_ref[...] = ((hf_ref[...] + hb_ref[...]) * gelu).astype(BF16)

    return pl.pallas_call(
        body, name="rec_fwd",
        out_shape=(jax.ShapeDtypeStruct((T, D_REC), F32), jax.ShapeDtypeStruct((T, D_REC), F32),
                   jax.ShapeDtypeStruct((T, D_REC), BF16), jax.ShapeDtypeStruct((4, T, D_REC), F32)),
        grid=(N_CG,),
        in_specs=[tok(0), tok(N_CG), per_ch(4), per_ch(1), wspec, per_ch(2), wspec, per_ch(2), per_ch(2),
                  const((REC_BLOCK, CG)), const((CG, CG))],
        out_specs=(tok(0), tok(0), tok(0), pl.BlockSpec((4, T, CG), lambda g: (0, 0, g))),
        scratch_shapes=[pltpu.VMEM((T, CG), F32)] * 2,
        compiler_params=_params(dimension_semantics=("parallel",)),
    )(uy, uy, conv_w, conv_b, w_a, b_a, w_i, b_i, lam,
      jnp.asarray(_dup_table(), BF16), jnp.asarray(_pair_mask()))


def _rec_bwd(uy, hf, hb, am, dyrec, conv_w, conv_b, w_a, b_a, w_i, b_i, lam, after):
    tok, per_ch, wspec, const = _rec_specs()

    def body(up_ref, yb_ref, hf_ref, hb_ref, am_ref, dy_ref, cw_ref, cb_ref, wa_ref, ba_ref, wi_ref, bi_ref,
             lam_ref, dup_ref, dupt_ref, half_ref, after_ref,
             duy_ref, dcw_ref, dcb_ref, dwa_ref, dba_ref, dwi_ref, dbi_ref, dlam_ref,
             a_s0, a_s1, dh_s, g_s0, g_s1):
        dup = dup_ref[...]
        dup_t = dupt_ref[...]
        same_half = half_ref[...] > 0.5
        taps = _conv_taps(up_ref[...])
        u = cb_ref[...]
        for j in range(4):
            u = u + taps[j] * cw_ref[j:j + 1, :]
        u16 = u.astype(BF16)
        gelu, dgelu = _gelu_and_grad(yb_ref[...])
        dy = dy_ref[...]
        duy_ref[1] = (dy * (hf_ref[...] + hb_ref[...]) * dgelu).astype(BF16)
        dh_s[...] = dy * gelu
        a_s0[...] = _shift_rows(am_ref[0], -1)
        a_s1[...] = _shift_rows(am_ref[2], 1)
        _scans([(a_s0, dh_s, g_s0, True), (a_s1, dh_s, g_s1, False)])
        du = jnp.zeros((T, CG), F32)
        for d, g_s in enumerate((g_s0, g_s1)):
            reverse = d == 1
            wa = _pair_block_diag(wa_ref[d, 0], dup, same_half)
            wi = _pair_block_diag(wi_ref[d, 0], dup, same_half)
            lam_d = lam_ref[d:d + 1, :]
            r = _sigmoid(_dot(u16, wa) + ba_ref[d:d + 1, :])
            ig = _sigmoid(_dot(u16, wi) + bi_ref[d:d + 1, :])
            sp = _softplus(-lam_d)
            a, mult = am_ref[2 * d], am_ref[2 * d + 1]
            mult2 = mult * mult
            g = g_s[...]
            h_prev = _shift_rows(hb_ref[...], -1) if reverse else _shift_rows(hf_ref[...], 1)
            da = g * h_prev
            dmult = g * (ig * u)
            dig = g * mult * u
            du = du + g * mult * ig
            dmult_dlog = jnp.where(mult2 > 0.0, -(a * a) * lax.rsqrt(mult2), 0.0)
            dlog_a = da * a + dmult * dmult_dlog
            dr = dlog_a * ((-LRU_C) * sp)
            dsp = jnp.sum(dlog_a * ((-LRU_C) * r), axis=0, keepdims=True)
            dlam_ref[d:d + 1, :] = dsp * (-_sigmoid(-lam_d))
            dga = dr * r * (1.0 - r)
            dgi = dig * ig * (1.0 - ig)
            dga16 = dga.astype(BF16)
            dgi16 = dgi.astype(BF16)
            du = du + _dot_nt(dga16, wa) + _dot_nt(dgi16, wi)
            dwa_ref[d, 0] = _dot_exact(jnp.where(same_half, _dot_tn(u16, dga16), 0.0), dup_t)
            dwi_ref[d, 0] = _dot_exact(jnp.where(same_half, _dot_tn(u16, dgi16), 0.0), dup_t)
            dba_ref[d:d + 1, :] = jnp.sum(dga, axis=0, keepdims=True)
            dbi_ref[d:d + 1, :] = jnp.sum(dgi, axis=0, keepdims=True)
        dcb_ref[...] = jnp.sum(du, axis=0, keepdims=True)
        for j in range(4):
            dcw_ref[j:j + 1, :] = jnp.sum(du * taps[j], axis=0, keepdims=True)
        dup_in = (_shift_rows(du, -2) * cw_ref[0:1, :] + _shift_rows(du, -1) * cw_ref[1:2, :]
                  + du * cw_ref[2:3, :] + _shift_rows(du, 1) * cw_ref[3:4, :])
        duy_ref[0] = dup_in.astype(BF16)

    wshape = jax.ShapeDtypeStruct((2, N_CG, CG, REC_BLOCK), F32)
    vec = lambda rows: jax.ShapeDtypeStruct((rows, D_REC), F32)
    dup_np = _dup_table()
    return pl.pallas_call(
        body, name="rec_bwd",
        out_shape=(jax.ShapeDtypeStruct((2, T, D_REC), BF16),
                   vec(4), vec(1), wshape, vec(2), wshape, vec(2), vec(2)),
        grid=(N_CG,),
        in_specs=[tok(0), tok(N_CG), tok(0), tok(0), pl.BlockSpec((4, T, CG), lambda g: (0, 0, g)), tok(0),
                  per_ch(4), per_ch(1), wspec, per_ch(2), wspec, per_ch(2), per_ch(2),
                  const((REC_BLOCK, CG)), const((CG, REC_BLOCK)), const((CG, CG)),
                  pl.BlockSpec(memory_space=pl.ANY)],
        out_specs=(pl.BlockSpec((2, T, CG), lambda g: (0, 0, g)),
                   per_ch(4), per_ch(1), wspec, per_ch(2), wspec, per_ch(2), per_ch(2)),
        scratch_shapes=[pltpu.VMEM((T, CG), F32)] * 5,
        compiler_params=_params(dimension_semantics=("parallel",)),
    )(uy, uy, hf, hb, am, dyrec, conv_w, conv_b, w_a, b_a, w_i, b_i, lam,
      jnp.asarray(dup_np, BF16), jnp.asarray(dup_np.T.copy()), jnp.asarray(_pair_mask()), after)


TM_MIX = 256


def _mix_specs():
    tok = lambda width, blk=0: pl.BlockSpec((TM_MIX, width), lambda i: (i, blk))
    full = lambda shape: pl.BlockSpec(shape, lambda i: (0, 0))
    return tok, full


def _mix_fwd(x, att, yrec, gg, w_att_o_t, w_rec_o, w_out):
    tok, full = _mix_specs()

    def body(x_ref, att_ref, yr_ref, ga_ref, gr_ref, wao_ref, wro_ref, wo_ref, x1_ref, mixed_ref):
        y_att = _dot_nt(att_ref[...], wao_ref[...])
        y_rec = _dot(yr_ref[...], wro_ref[...])
        mixed = (_sigmoid(ga_ref[...]) * y_att + _sigmoid(gr_ref[...]) * y_rec).astype(BF16)
        mixed_ref[...] = mixed
        x1_ref[...] = x_ref[...] + _dot(mixed, wo_ref[...])

    return pl.pallas_call(
        body, name="mix_fwd",
        out_shape=(jax.ShapeDtypeStruct((T, D), F32), jax.ShapeDtypeStruct((T, D), BF16)),
        grid=(T // TM_MIX,),
        in_specs=[tok(D), tok(D_ATT), tok(D_REC), tok(D, 0), tok(D, 1),
                  full((D, D_ATT)), full((D_REC, D)), full((D, D))],
        out_specs=(tok(D), tok(D)),
        compiler_params=_params(dimension_semantics=("parallel",)),
    )(x, att, yrec, gg, gg, w_att_o_t, w_rec_o, w_out)


def _mix_bwd(dx1, att, yrec, gg, w_att_o_t, w_rec_o, w_out, after):
    tok, full = _mix_specs()

    def body(dx_ref, att_ref, yr_ref, ga_ref, gr_ref, wao_ref, wro_ref, wo_ref, after_ref,
             dgg_ref, dya_ref, dyr_ref, datt_ref, dyrp_ref):
        dmixed = _dot_nt(dx_ref[...].astype(BF16), wo_ref[...])
        y_att = _dot_nt(att_ref[...], wao_ref[...])
        y_rec = _dot(yr_ref[...], wro_ref[...])
        sa = _sigmoid(ga_ref[...])
        sr = _sigmoid(gr_ref[...])
        dgg_ref[0] = (dmixed * y_att * sa * (1.0 - sa)).astype(BF16)
        dgg_ref[1] = (dmixed * y_rec * sr * (1.0 - sr)).astype(BF16)
        dya = (dmixed * sa).astype(BF16)
        dyr = (dmixed * sr).astype(BF16)
        dya_ref[...] = dya
        dyr_ref[...] = dyr
        datt_ref[...] = _dot(dya, wao_ref[...]).astype(BF16)
        dyrp_ref[...] = _dot_nt(dyr, wro_ref[...])

    return pl.pallas_call(
        body, name="mix_bwd",
        out_shape=(jax.ShapeDtypeStruct((2, T, D), BF16),
                   jax.ShapeDtypeStruct((T, D), BF16), jax.ShapeDtypeStruct((T, D), BF16),
                   jax.ShapeDtypeStruct((T, D_ATT), BF16), jax.ShapeDtypeStruct((T, D_REC), F32)),
        grid=(T // TM_MIX,),
        in_specs=[tok(D), tok(D_ATT), tok(D_REC), tok(D, 0), tok(D, 1),
                  full((D, D_ATT)), full((D_REC, D)), full((D, D)), pl.BlockSpec(memory_space=pl.ANY)],
        out_specs=(pl.BlockSpec((2, TM_MIX, D), lambda i: (0, i, 0)),
                   tok(D), tok(D), tok(D_ATT), tok(D_REC)),
        compiler_params=_params(dimension_semantics=("parallel",)),
    )(dx1, att, yrec, gg, gg, w_att_o_t, w_rec_o, w_out, after)


TM_FFN = 256
FF_CHUNK = 1024


def _ffn_loss(x1, target, g2, gf, w_ff1_t, w_ff2):
    n_chunks = D_FF // FF_CHUNK

    def body(x1_ref, tg_ref, g2_ref, gf_ref, w1_hbm, w2_hbm,
             loss_ref, dx1_ref, h2_ref, act_ref, dpre_ref, dx2_ref, dg2_ref, dgf_ref,
             w1, w2, relu_s):
        i = pl.program_id(0)

        @pl.when(i == 0)
        def _():
            pltpu.sync_copy(w1_hbm, w1)
            pltpu.sync_copy(w2_hbm, w2)
            loss_ref[...] = jnp.zeros_like(loss_ref)
            dg2_ref[...] = jnp.zeros_like(dg2_ref)
            dgf_ref[...] = jnp.zeros_like(dgf_ref)

        x1v = x1_ref[...]
        r2 = lax.rsqrt(jnp.mean(x1v * x1v, axis=-1, keepdims=True) + EPS)
        xh2 = x1v * r2
        h2 = (xh2 * g2_ref[...]).astype(BF16)
        h2_ref[...] = h2
        x2 = x1v
        for c in range(n_chunks):
            ff = slice(c * FF_CHUNK, (c + 1) * FF_CHUNK)
            rl = jnp.maximum(_dot_nt(h2, w1[ff, :]), 0.0)
            relu_s[:, ff] = rl
            act = (rl * rl).astype(BF16)
            act_ref[:, ff] = act
            x2 = x2 + _dot(act, w2[ff, :])
        r3 = lax.rsqrt(jnp.mean(x2 * x2, axis=-1, keepdims=True) + EPS)
        xh3 = x2 * r3
        err = xh3 * gf_ref[...] - tg_ref[...]
        loss_ref[...] += 0.5 * jnp.sum(jnp.mean(err * err, axis=-1, keepdims=True))
        dy = err * (1.0 / D)
        dgf_ref[...] += jnp.sum(dy * xh3, axis=0, keepdims=True)
        dx2 = _rms_bwd(dy, xh3, r3, gf_ref[...])
        dx2_16 = dx2.astype(BF16)
        dx2_ref[...] = dx2_16
        dh2 = jnp.zeros((TM_FFN, D), F32)
        for c in range(n_chunks):
            ff = slice(c * FF_CHUNK, (c + 1) * FF_CHUNK)
            dpre = (_dot_nt(dx2_16, w2[ff, :]) * (2.0 * relu_s[:, ff])).astype(BF16)
            dpre_ref[:, ff] = dpre
            dh2 = dh2 + _dot(dpre, w1[ff, :])
        dg2_ref[...] += jnp.sum(dh2 * xh2, axis=0, keepdims=True)
        dx1_ref[...] = dx2 + _rms_bwd(dh2, xh2, r2, g2_ref[...])

    tok = lambda width: pl.BlockSpec((TM_FFN, width), lambda i: (i, 0))
    vec = pl.BlockSpec((1, D), lambda i: (0, 0))
    hbm = pl.BlockSpec(memory_space=pl.ANY)
    return pl.pallas_call(
        body, name="ffn_loss",
        out_shape=(jax.ShapeDtypeStruct((8, 128), F32), jax.ShapeDtypeStruct((T, D), F32),
                   jax.ShapeDtypeStruct((T, D), BF16), jax.ShapeDtypeStruct((T, D_FF), BF16),
                   jax.ShapeDtypeStruct((T, D_FF), BF16), jax.ShapeDtypeStruct((T, D), BF16),
                   jax.ShapeDtypeStruct((1, D), F32), jax.ShapeDtypeStruct((1, D), F32)),
        grid=(T // TM_FFN,),
        in_specs=[tok(D), tok(D), vec, vec, hbm, hbm],
        out_specs=(pl.BlockSpec((8, 128), lambda i: (0, 0)), tok(D), tok(D), tok(D_FF), tok(D_FF), tok(D),
                   vec, vec),
        scratch_shapes=[pltpu.VMEM((D_FF, D), BF16), pltpu.VMEM((D_FF, D), BF16),
                        pltpu.VMEM((TM_FFN, D_FF), F32)],
        compiler_params=_params(dimension_semantics=("arbitrary",)),
    )(x1, target, g2, gf, w_ff1_t, w_ff2)


def _local_step(x, target, p, late_weights, late_weights_ready, reduce_first, reduce_early, reduce_late):
    bias = _rpb_rows(p["rpb"])
    pairs = lambda w: w.reshape(2, N_CG, CG, REC_BLOCK)
    w_a, w_i = pairs(p["w_rg_a"]), pairs(p["w_rg_i"])
    rec_params = (p["conv_w"], p["conv_b"], w_a, p["b_rg_a"], w_i, p["b_rg_i"], p["lru_lambda"])

    qkv, uy, gg, h = _in_proj(x, p["ln1_g"], p["w_in_t"], p["b_in"], p["later_weights_started"])
    hf, hb, yrec, am = _rec_fwd(uy, *rec_params)
    att = _att_fwd(qkv, bias, late_weights(yrec, 0))
    p = {**p, **late_weights_ready(late_weights(att, 1), 0)}
    x1, mixed = _mix_fwd(x, att, yrec, gg, p["w_att_o_t"], p["w_rec_o"], p["w_out"])
    p = {**p, **late_weights_ready(x1, 1)}
    loss8, dx1, h2, act, dpre, dx2, g_ln2, g_lnf = _ffn_loss(
        x1, target, p["ln2_g"], p["lnf_g"], p["w_ff1_t"], p["w_ff2"])

    grads = {"ln2_g": g_ln2, "lnf_g": g_lnf,
             "w_ff1_t": _matmul(dpre, h2, "tn", BF16, "g_w_ff1"),
             "w_ff2": _matmul(act, dx2, "tn", BF16, "g_w_ff2")}
    dgg, dya, dyr, datt, dyrp = _mix_bwd(dx1, att, yrec, gg, p["w_att_o_t"], p["w_rec_o"], p["w_out"],
                                         reduce_first(grads, None))
    duy, g_cw, g_cb, g_wa, g_ba, g_wi, g_bi, g_lam = _rec_bwd(uy, hf, hb, am, dyrp, *rec_params,
                                                              reduce_first(None, dgg))
    blocks = lambda g: g.reshape(2, N_REC_BLOCKS, REC_BLOCK, REC_BLOCK)
    grads.update({
        "w_att_o_t": _matmul(dya, att, "tn", BF16, "g_w_att_o"),
        "conv_w": g_cw, "conv_b": g_cb, "w_rg_a": blocks(g_wa), "b_rg_a": g_ba,
        "w_rg_i": blocks(g_wi), "b_rg_i": g_bi, "lru_lambda": g_lam,
        "w_rec_o": _matmul(yrec, dyr, "tn", BF16, "g_w_rec_o"),
        "w_out": _matmul(mixed, dx1, "tn", BF16, "g_w_out"),
    })
    dqkv, gbias = _att_bwd(qkv, bias, datt, reduce_early(grads))
    dz = (dqkv, duy, dgg)
    g_w_in_t, g_b_in = _grad_w_in(dz, h)
    grads.update(w_in_t=g_w_in_t, b_in=g_b_in)
    grad_x, g_ln1 = _dh_norm1_bwd(dz, p["w_in_t"], x, p["ln1_g"], dx1, reduce_late(grads))
    grads.update(ln1_g=g_ln1, rpb=_rpb_fold(gbias))
    return loss8[0:1, 0:1], grad_x, grads


MESH_ID = pl.DeviceIdType.MESH
ANY = pl.BlockSpec(memory_space=pl.ANY)

CHAN_BLOCK_ROWS = 32
GATE_ROWS = 2 * 2 * N_REC_BLOCKS * REC_BLOCK * REC_BLOCK // (N_DEV * D)
SECTIONS = (("w_in_t", 704, D), ("w_rec_o", 128, D), ("w_out", 128, D), ("w_ff1_t", 512, D),
            ("w_ff2", 512, D), ("chan", CHAN_BLOCK_ROWS, D), ("w_att_o_t", 128, D_ATT),
            ("gates", GATE_ROWS, D))
N_SEC = len(SECTIONS)
N_CHAN_ROWS = 10
CHAN = (("conv_w", 4), ("b_rg_a", 2), ("b_rg_i", 2), ("lru_lambda", 2))


def _position():
    return lax.axis_index("x"), lax.axis_index("y"), lax.axis_index("c")


def _other_chips(x, y):
    return [(1 - x, y), (x, 1 - y), (1 - x, 1 - y)]


PASS_ON_IDS, PAIR_EARLY_ID, PAIR_LATE_ID, PAIR_FIRST_ID, SMALL_PASS_ON_ID = (1, 4), 2, 3, 5, 6


def _pair_handshake(x, y, c):
    barrier = pltpu.get_barrier_semaphore()
    pl.semaphore_signal(barrier, inc=1, device_id=(x, y, 1 - c), device_id_type=MESH_ID)
    pl.semaphore_wait(barrier, 1)


def _block_of(ref, dev, rows):
    return ref.at[pl.ds(pl.multiple_of(dev * rows, 16), rows)]


def _all_gather(shards, name, pieces):
    ns = len(shards)
    parts = [(s, j * (a.shape[0] // n), a.shape[0] // n)
             for s, (a, n) in enumerate(zip(shards, pieces)) for j in range(n)]
    np_ = len(parts)

    def body(*refs):
        x_refs, out_refs = refs[:ns], refs[ns:2 * ns]
        send_sems, recv_sems, local_sems = refs[2 * ns:]
        x, y, c = _position()
        me, sibling = (x, y, c), (x, y, 1 - c)
        x_nbr, y_nbr, diagonal = _other_chips(x, y)
        north = c == 1
        relay_from = (jnp.where(north, x_nbr[0], y_nbr[0]), jnp.where(north, x_nbr[1], y_nbr[1]))
        relay_to = (jnp.where(north, y_nbr[0], x_nbr[0]), jnp.where(north, y_nbr[1], x_nbr[1]))

        def rows(v, px, py, pc):
            s, r0, r = parts[v]
            start = (4 * px + 2 * py + pc) * shards[s].shape[0] + r0
            return out_refs[s].at[pl.ds(pl.multiple_of(start, 16), r)]

        def own(v):
            s, r0, r = parts[v]
            return x_refs[s].at[pl.ds(r0, r)]

        def copy(k, v, block, to, from_shard=False):
            return pltpu.make_async_remote_copy(
                src_ref=own(v) if from_shard else rows(v, *block), dst_ref=rows(v, *block),
                send_sem=send_sems.at[k * np_ + v], recv_sem=recv_sems.at[k * np_ + v],
                device_id=to, device_id_type=MESH_ID)

        sections = range(np_)
        mine = [pltpu.make_async_copy(own(v), rows(v, *me), local_sems.at[v]) for v in sections]
        sent = [copy(k, v, me, to, True) for v in sections
                for k, to in enumerate((sibling, (*x_nbr, c), (*y_nbr, c)))]
        for cp in mine + sent:
            cp.start()
        for s in sections:
            copy(1, s, (*x_nbr, c), me).wait_recv()
            copy(2, s, (*y_nbr, c), me).wait_recv()
            sent += [copy(3, s, (*relay_from, c), (*relay_to, c)),
                     copy(4, s, (*x_nbr, c), sibling), copy(5, s, (*y_nbr, c), sibling)]
            for cp in sent[-3:]:
                cp.start()
        for s in sections:
            copy(3, s, (*diagonal, c), me).wait_recv()
            sent.append(copy(6, s, (*diagonal, c), sibling))
            sent[-1].start()
        for s in sections:
            copy(0, s, sibling, me).wait_recv()
            for k, chip in ((4, x_nbr), (5, y_nbr), (6, diagonal)):
                copy(k, s, (*chip, 1 - c), me).wait_recv()
        for cp in sent:
            cp.wait_send()
        for cp in mine:
            cp.wait()

    return pl.pallas_call(
        body, name=name,
        out_shape=tuple(jax.ShapeDtypeStruct((N_DEV * s.shape[0], s.shape[1]), s.dtype) for s in shards),
        in_specs=[ANY] * ns,
        out_specs=(ANY,) * ns,
        scratch_shapes=[pltpu.SemaphoreType.DMA((7 * np_,)), pltpu.SemaphoreType.DMA((7 * np_,)),
                        pltpu.SemaphoreType.DMA((np_,))],
    )(*shards)


HBM = pl.BlockSpec(memory_space=pltpu.HBM)
SEM = pl.BlockSpec(memory_space=pltpu.SEMAPHORE)
EFFECT = pltpu.SideEffectType.DATAFLOW_SIDE_EFFECTING


def _in_hbm(a):
    return pltpu.with_memory_space_constraint(a, pltpu.HBM)


def _first_hop_copies(rows, which, x_refs, zones, send_sems, recv_sems):
    ns = len(rows)
    x, y, c = _position()
    targets = [(x, y, 1 - c)] + [(cx, cy, c) for cx, cy in _other_chips(x, y)]
    return [pltpu.make_async_remote_copy(
        src_ref=x_refs[i], dst_ref=_block_of(zones[i], 4 * x + 2 * y + c, rows[s]),
        send_sem=send_sems.at[k * ns + s], recv_sem=recv_sems.at[k * ns + s],
        device_id=to, device_id_type=MESH_ID)
        for k, to in enumerate(targets) for i, s in enumerate(which)]


def _after_all(arrays, name):
    def body(*refs):
        refs[-1][...] = jnp.zeros_like(refs[-1])

    return pl.pallas_call(
        body, name=name,
        out_shape=jax.ShapeDtypeStruct((8, LANES), F32),
        in_specs=[pl.BlockSpec(memory_space=pl.ANY)] * len(arrays),
        out_specs=pl.BlockSpec(memory_space=pltpu.VMEM),
    )(*arrays)


def _own_blocks_placed(shards, after, name):
    ns = len(shards)
    x, y, c = _position()
    me = jnp.reshape(4 * x + 2 * y + c, (1,)).astype(jnp.int32)
    tokens = [] if after is None else [after]

    def body(me_ref, *refs):
        for s in range(ns):
            refs[ns + len(tokens) + s][...] = refs[s][...]

    return pl.pallas_call(
        body, name=name,
        out_shape=tuple(jax.ShapeDtypeStruct((N_DEV * s.shape[0], s.shape[1]), s.dtype) for s in shards),
        grid_spec=pltpu.PrefetchScalarGridSpec(
            num_scalar_prefetch=1, grid=(1,),
            in_specs=[pl.BlockSpec(s.shape, lambda i, me: (0, 0)) for s in shards] + [ANY] * len(tokens),
            out_specs=tuple(pl.BlockSpec(s.shape, lambda i, me: (me[0], 0)) for s in shards)),
        compiler_params=_params(dimension_semantics=("arbitrary",)),
    )(me, *shards, *tokens)


def _gather_start(shards, after, name):
    ns = len(shards)
    zones = _own_blocks_placed(shards, after, name + "_own_blocks")

    def body(*refs):
        for cp in _first_hop_copies([s.shape[0] for s in shards], range(ns), refs[:ns], refs[ns:2 * ns],
                                    refs[2 * ns], refs[2 * ns + 1]):
            cp.start()
        refs[-1][...] = jnp.zeros_like(refs[-1])

    out = pl.pallas_call(
        body, name=name,
        out_shape=(pltpu.SemaphoreType.DMA((4 * ns,)), pltpu.SemaphoreType.DMA((4 * ns,)),
                   *[pltpu.HBM(a.shape, a.dtype) for a in (*shards, *zones)],
                   jax.ShapeDtypeStruct((8, LANES), F32)),
        in_specs=[HBM] * (2 * ns),
        out_specs=(SEM, SEM, *[HBM] * (2 * ns), pl.BlockSpec(memory_space=pltpu.VMEM)),
        input_output_aliases={i: 2 + i for i in range(2 * ns)},
        compiler_params=pltpu.CompilerParams(has_side_effects=EFFECT),
    )(*[_in_hbm(a) for a in shards], *[_in_hbm(a) for a in zones])
    return out[0], out[1], out[2:2 + ns], out[2 + ns:2 + 2 * ns], out[-1]


def _gather_wait(send_sems, recv_sems, rows, which, shards, zones, after, name):
    ns = len(shards)

    def body(*refs):
        for cp in _first_hop_copies(rows, which, refs[:ns], refs[ns:2 * ns], refs[2 * ns], refs[2 * ns + 1]):
            cp.wait_send()
            cp.wait_recv()

    out = pl.pallas_call(
        body, name=name,
        out_shape=tuple(pltpu.HBM(a.shape, a.dtype) for a in (*shards, *zones)),
        in_specs=[HBM] * (2 * ns) + [SEM, SEM, ANY],
        out_specs=(HBM,) * (2 * ns),
        input_output_aliases={i: i for i in range(2 * ns)},
        compiler_params=pltpu.CompilerParams(has_side_effects=EFFECT),
    )(*shards, *zones, send_sems, recv_sems, after)
    return out[:ns], out[ns:]


def _pass_on_copies(rows, in_refs, out_refs, send_sems, recv_sems):
    ns = len(rows)
    x, y, c = _position()
    return [pltpu.make_async_remote_copy(
        src_ref=_block_of(in_refs[s], 4 * cx + 2 * cy + c, rows[s]),
        dst_ref=_block_of(out_refs[s], 4 * cx + 2 * cy + c, rows[s]),
        send_sem=send_sems.at[j * ns + s], recv_sem=recv_sems.at[j * ns + s],
        device_id=(x, y, 1 - c), device_id_type=MESH_ID)
        for j, (cx, cy) in enumerate(_other_chips(x, y)) for s in range(ns)]


def _pass_on_start(rows, zones, barrier_id, name):
    ns = len(zones)

    def body(*refs):
        _pair_handshake(*_position())
        for cp in _pass_on_copies(rows, refs[:ns], refs[:ns], refs[ns], refs[ns + 1]):
            cp.start()
        refs[-1][...] = jnp.zeros_like(refs[-1])

    out = pl.pallas_call(
        body, name=name,
        out_shape=(pltpu.SemaphoreType.DMA((3 * ns,)), pltpu.SemaphoreType.DMA((3 * ns,)),
                   *[pltpu.HBM(z.shape, z.dtype) for z in zones], jax.ShapeDtypeStruct((8, LANES), F32)),
        in_specs=[HBM] * ns,
        out_specs=(SEM, SEM, *[HBM] * ns, pl.BlockSpec(memory_space=pltpu.VMEM)),
        input_output_aliases={i: 2 + i for i in range(ns)},
        compiler_params=pltpu.CompilerParams(has_side_effects=EFFECT, collective_id=barrier_id),
    )(*[_in_hbm(z) for z in zones])
    return out[0], out[1], out[2:2 + ns], out[-1]


def _pass_on_wait(rows, send_sems, recv_sems, zones, after, name):
    ns = len(zones)

    def body(*refs):
        for cp in _pass_on_copies(rows, refs[:ns], refs[:ns], refs[ns], refs[ns + 1]):
            cp.wait_send()
            cp.wait_recv()

    return pl.pallas_call(
        body, name=name,
        out_shape=tuple(pltpu.HBM(z.shape, z.dtype) for z in zones),
        in_specs=[HBM] * ns + [SEM, SEM, ANY],
        out_specs=(HBM,) * ns,
        input_output_aliases={i: i for i in range(ns)},
        compiler_params=pltpu.CompilerParams(has_side_effects=EFFECT),
    )(*zones, send_sems, recv_sems, after)


def _gather_pass_on(rows, zones, barrier_id, name):
    ns = len(zones)

    def body(*refs):
        _pair_handshake(*_position())
        copies = _pass_on_copies(rows, refs[:ns], refs[ns:2 * ns], *refs[2 * ns:])
        for cp in copies:
            cp.start()
        for cp in copies:
            cp.wait_recv()
        for cp in copies:
            cp.wait_send()

    return pl.pallas_call(
        body, name=name,
        out_shape=tuple(jax.ShapeDtypeStruct(z.shape, z.dtype) for z in zones),
        in_specs=[ANY] * ns, out_specs=(ANY,) * ns,
        input_output_aliases={i: i for i in range(ns)},
        scratch_shapes=[pltpu.SemaphoreType.DMA((3 * ns,)), pltpu.SemaphoreType.DMA((3 * ns,))],
        compiler_params=pltpu.CompilerParams(collective_id=barrier_id),
    )(*zones)


def _pair_copies(sections, g_refs, land, send_sems, recv_sems):
    ns = len(sections)
    x, y, c = _position()
    return [pltpu.make_async_remote_copy(
        src_ref=_block_of(g_refs[s], 2 * k + 1 - c, rows), dst_ref=land[s].at[k],
        send_sem=send_sems.at[k * ns + s], recv_sem=recv_sems.at[k * ns + s],
        device_id=(x, y, 1 - c), device_id_type=MESH_ID)
        for k in range(N_CHIPS) for s, (_, rows, _) in enumerate(sections)]


def _pair_exchange_start(sections, grads, barrier_id, name):
    ns = len(sections)

    def body(*refs):
        _pair_handshake(*_position())
        for cp in _pair_copies(sections, refs[:ns], refs[ns:2 * ns], refs[2 * ns], refs[2 * ns + 1]):
            cp.start()
        refs[-1][...] = jnp.zeros_like(refs[-1])

    zones = [lax.empty((N_CHIPS, rows, cols), BF16) for _, rows, cols in sections]
    n = N_CHIPS * ns
    out = pl.pallas_call(
        body, name=name,
        out_shape=(pltpu.SemaphoreType.DMA((n,)), pltpu.SemaphoreType.DMA((n,)),
                   *[pltpu.HBM(a.shape, a.dtype) for a in (*grads, *zones)],
                   jax.ShapeDtypeStruct((8, LANES), F32)),
        in_specs=[HBM] * (2 * ns),
        out_specs=(SEM, SEM, *[HBM] * (2 * ns), pl.BlockSpec(memory_space=pltpu.VMEM)),
        input_output_aliases={i: 2 + i for i in range(2 * ns)},
        compiler_params=pltpu.CompilerParams(has_side_effects=EFFECT, collective_id=barrier_id),
    )(*[_in_hbm(a) for a in grads], *[_in_hbm(a) for a in zones])
    return out[0], out[1], out[2:2 + ns], out[2 + ns:2 + 2 * ns], out[-1]


def _pair_exchange_wait(sections, send_sems, recv_sems, grads, zones, after, name):
    ns = len(sections)

    def body(*refs):
        for cp in _pair_copies(sections, refs[:ns], refs[ns:2 * ns], refs[2 * ns], refs[2 * ns + 1]):
            cp.wait_send()
            cp.wait_recv()

    out = pl.pallas_call(
        body, name=name,
        out_shape=tuple(pltpu.HBM(a.shape, a.dtype) for a in (*grads, *zones)),
        in_specs=[HBM] * (2 * ns) + [SEM, SEM, ANY],
        out_specs=(HBM,) * (2 * ns),
        input_output_aliases={i: i for i in range(2 * ns)},
        compiler_params=pltpu.CompilerParams(has_side_effects=EFFECT),
    )(*grads, *zones, send_sems, recv_sems, after)
    return out[:ns], out[ns:]


def _pair_add(sections, grads, got, core, name):
    ns = len(sections)

    def body(core_ref, *refs):
        g_refs, got_refs, p_refs = refs[:ns], refs[ns:2 * ns], refs[2 * ns:]
        for s in range(ns):
            p_refs[s][0] = (g_refs[s][...].astype(F32) + got_refs[s][0].astype(F32)).astype(BF16)

    slot = [pl.BlockSpec((1, rows, cols), lambda k, c: (k, 0, 0)) for _, rows, cols in sections]
    return pl.pallas_call(
        body, name=name,
        out_shape=tuple(jax.ShapeDtypeStruct((N_CHIPS, rows, cols), BF16) for _, rows, cols in sections),
        grid_spec=pltpu.PrefetchScalarGridSpec(
            num_scalar_prefetch=1, grid=(N_CHIPS,),
            in_specs=[pl.BlockSpec((rows, cols), lambda k, c: (2 * k + c[0], 0)) for _, rows, cols in sections]
            + slot,
            out_specs=tuple(slot)),
        compiler_params=_params(dimension_semantics=("parallel",)),
    )(core, *grads, *got)


def _chip_copies(sections, p_refs, land, send_sems, recv_sems):
    ns = len(sections)
    x, y, c = _position()
    return [pltpu.make_async_remote_copy(
        src_ref=p_refs[s].at[2 * cx + cy], dst_ref=land[s].at[j],
        send_sem=send_sems.at[j * ns + s], recv_sem=recv_sems.at[j * ns + s],
        device_id=(cx, cy, c), device_id_type=MESH_ID)
        for j, (cx, cy) in enumerate(_other_chips(x, y)) for s in range(ns)]


def _chip_exchange(sections, parts, name):
    ns = len(sections)

    def body(*refs):
        copies = _chip_copies(sections, refs[:ns], refs[ns:2 * ns], *refs[2 * ns:])
        for cp in copies:
            cp.start()
        for cp in copies:
            cp.wait_recv()
        for cp in copies:
            cp.wait_send()

    n = 3 * ns
    return pl.pallas_call(
        body, name=name,
        out_shape=tuple(jax.ShapeDtypeStruct((3, rows, cols), BF16) for _, rows, cols in sections),
        in_specs=[ANY] * ns, out_specs=(ANY,) * ns,
        scratch_shapes=[pltpu.SemaphoreType.DMA((n,)), pltpu.SemaphoreType.DMA((n,))],
    )(*parts)


def _chip_exchange_start(sections, parts, name):
    ns = len(sections)

    def body(*refs):
        p_refs, land = refs[:ns], refs[ns:2 * ns]
        send_sems, recv_sems = refs[2 * ns], refs[2 * ns + 1]
        token = refs[-1]
        for cp in _chip_copies(sections, p_refs, land, send_sems, recv_sems):
            cp.start()
        token[...] = jnp.zeros_like(token)

    zones = [lax.empty((3, rows, cols), BF16) for _, rows, cols in sections]
    out = pl.pallas_call(
        body, name=name,
        out_shape=(pltpu.SemaphoreType.DMA((3 * ns,)), pltpu.SemaphoreType.DMA((3 * ns,)),
                   *[pltpu.HBM(a.shape, a.dtype) for a in parts], *[pltpu.HBM(a.shape, a.dtype) for a in zones],
                   jax.ShapeDtypeStruct((8, LANES), F32)),
        in_specs=[HBM] * (2 * ns),
        out_specs=(SEM, SEM, *[HBM] * (2 * ns), pl.BlockSpec(memory_space=pltpu.VMEM)),
        input_output_aliases={i: 2 + i for i in range(2 * ns)},
        compiler_params=pltpu.CompilerParams(has_side_effects=EFFECT),
    )(*[_in_hbm(a) for a in parts], *[_in_hbm(a) for a in zones])
    return out[0], out[1], out[2:2 + ns], out[2 + ns:2 + 2 * ns], out[-1]


def _chip_exchange_wait(sections, send_sems, recv_sems, parts, zones, after, name):
    ns = len(sections)

    def body(*refs):
        p_refs, land = refs[:ns], refs[ns:2 * ns]
        for cp in _chip_copies(sections, p_refs, land, refs[2 * ns], refs[2 * ns + 1]):
            cp.wait_send()
            cp.wait_recv()

    out = pl.pallas_call(
        body, name=name,
        out_shape=tuple(pltpu.HBM(a.shape, a.dtype) for a in (*parts, *zones)),
        in_specs=[HBM] * (2 * ns) + [SEM, SEM, ANY],
        out_specs=(HBM,) * (2 * ns),
        input_output_aliases={i: i for i in range(2 * ns)},
        compiler_params=pltpu.CompilerParams(has_side_effects=EFFECT),
    )(*parts, *zones, send_sems, recv_sems, after)
    return out[:ns], out[ns:]


def _grad_finish(sections, parts, far, chip, name):
    ns = len(sections)

    def body(chip_ref, *refs):
        p_refs, b_refs, g_refs = refs[:ns], refs[ns:2 * ns], refs[2 * ns:]
        for s in range(ns):
            g = p_refs[s][0].astype(F32)
            for j in range(3):
                g = g + b_refs[s][j].astype(F32)
            g_refs[s][...] = g

    half = [(rows // 2, cols) for _, rows, cols in sections]
    return pl.pallas_call(
        body, name=name,
        out_shape=tuple(jax.ShapeDtypeStruct((rows, cols), F32) for _, rows, cols in sections),
        grid_spec=pltpu.PrefetchScalarGridSpec(
            num_scalar_prefetch=1, grid=(2,),
            in_specs=[pl.BlockSpec((1, r, c), lambda i, chip: (chip[0], i, 0)) for r, c in half]
            + [pl.BlockSpec((3, r, c), lambda i, chip: (0, i, 0)) for r, c in half],
            out_specs=tuple(pl.BlockSpec((r, c), lambda i, chip: (i, 0)) for r, c in half)),
        compiler_params=_params(dimension_semantics=("parallel",)),
    )(chip, *parts, *far)


def _sum_devices(parts, rows, name):
    cols = parts.shape[1]
    tr = rows // 2

    def body(*refs):
        s = refs[0][...].astype(F32)
        for d in range(1, N_DEV):
            s = s + refs[d][...].astype(F32)
        refs[N_DEV][...] = s

    return pl.pallas_call(
        body, name=name,
        out_shape=jax.ShapeDtypeStruct((rows, cols), F32),
        grid=(2,),
        in_specs=[pl.BlockSpec((tr, cols), lambda i, d=d: (2 * d + i, 0)) for d in range(N_DEV)],
        out_specs=pl.BlockSpec((tr, cols), lambda i: (i, 0)),
        compiler_params=_params(dimension_semantics=("parallel",)),
    )(*([parts] * N_DEV))


def _adamw_step(w_ref, g_ref, m_ref, v_ref, d_ref, nm_ref, nv_ref):
    c1 = 1.0 / (1.0 - ADAM_B1 ** ADAM_STEP)
    c2 = 1.0 / (1.0 - ADAM_B2 ** ADAM_STEP)
    gv = g_ref[...]
    nm = ADAM_B1 * m_ref[...] + (1.0 - ADAM_B1) * gv
    nv = ADAM_B2 * v_ref[...] + (1.0 - ADAM_B2) * (gv * gv)
    nm_ref[...] = nm
    nv_ref[...] = nv
    d_ref[...] = (-ADAM_LR) * ((nm * c1) / (jnp.sqrt(nv * c2) + ADAM_EPS) + ADAM_WD * w_ref[...])


def _adamw_small(params, name):
    n = len(params)

    def body(*refs):
        for k in range(n):
            _adamw_step(*refs[4 * k:4 * k + 4], *refs[4 * n + 3 * k:4 * n + 3 * k + 3])

    out = pl.pallas_call(
        body, name=name,
        out_shape=tuple(jax.ShapeDtypeStruct(p[0].shape, F32) for p in params for _ in range(3)),
    )(*[a for p in params for a in p])
    return [out[3 * k:3 * k + 3] for k in range(n)]


def _adamw(w, g, m, v, name, after=None):
    rows, cols = w.shape
    tr = rows
    while tr * cols * 4 > (1 << 20) and tr % 16 == 0:
        tr //= 2
    tokens = [] if after is None else [after]

    def body(*refs):
        _adamw_step(*refs[:4], *refs[4 + len(tokens):])

    spec = pl.BlockSpec((tr, cols), lambda i: (i, 0))
    shape = jax.ShapeDtypeStruct((rows, cols), F32)
    return pl.pallas_call(
        body, name=name,
        out_shape=(shape, shape, shape),
        grid=(rows // tr,),
        in_specs=[spec] * 4 + [ANY] * len(tokens), out_specs=(spec,) * 3,
        compiler_params=_params(dimension_semantics=("parallel",)),
    )(w, g, m, v, *tokens)


NAMES = ("ln1_g", "w_in", "b_in", "rpb", "w_att_o", "conv_w", "conv_b", "w_rg_a", "b_rg_a", "w_rg_i",
         "b_rg_i", "lru_lambda", "w_rec_o", "w_out", "ln2_g", "w_ff1", "w_ff2", "lnf_g")
TRANSPOSED = {"w_in": "w_in_t", "w_att_o": "w_att_o_t", "w_ff1": "w_ff1_t"}
ROW_SHARDED = ("w_rec_o", "w_out", "w_ff2")
REPLICATED = (("ln1_g", (1, D)), ("b_in", (1, D_IN)), ("rpb", (N_HEADS * N_RPB_R, N_RPB_C)),
              ("conv_b", (1, D_REC)), ("w_rg_a", (2 * N_REC_BLOCKS * REC_BLOCK, REC_BLOCK)),
              ("w_rg_i", (2 * N_REC_BLOCKS * REC_BLOCK, REC_BLOCK)), ("ln2_g", (1, D)), ("lnf_g", (1, D)))
GATE_BLOCKS = ("w_rg_a", "w_rg_i")
SMALL_ROWS = 112


def _chan_bits(vectors):
    chan = jnp.concatenate(vectors, axis=0)
    bits = lax.bitcast_convert_type(chan, BF16).reshape(-1)
    return jnp.pad(bits, (0, CHAN_BLOCK_ROWS * D - bits.shape[0])).reshape(CHAN_BLOCK_ROWS, D)


def _chan_from_bits(gathered):
    bits = gathered.reshape(N_DEV, CHAN_BLOCK_ROWS * D)[:, :2 * N_CHAN_ROWS * LANES]
    chan = lax.bitcast_convert_type(bits.reshape(N_DEV, N_CHAN_ROWS, LANES, 2), F32)
    return chan.transpose(1, 0, 2).reshape(N_CHAN_ROWS, D)


def kernel(x, ln1_g, w_in, b_in, rpb, w_att_o, conv_w, conv_b, w_rg_a, b_rg_a, w_rg_i, b_rg_i, lru_lambda, w_rec_o, w_out, ln2_g, w_ff1, w_ff2, lnf_g, loss_target, m_ln1_g, m_w_in, m_b_in, m_rpb, m_w_att_o, m_conv_w, m_conv_b, m_w_rg_a, m_b_rg_a, m_w_rg_i, m_b_rg_i, m_lru_lambda, m_w_rec_o, m_w_out, m_ln2_g, m_w_ff1, m_w_ff2, m_lnf_g, v_ln1_g, v_w_in, v_b_in, v_rpb, v_w_att_o, v_conv_w, v_conv_b, v_w_rg_a, v_b_rg_a, v_w_rg_i, v_b_rg_i, v_lru_lambda, v_w_rec_o, v_w_out, v_ln2_g, v_w_ff1, v_w_ff2, v_lnf_g):
    w = dict(zip(NAMES, (ln1_g, w_in, b_in, rpb, w_att_o, conv_w, conv_b, w_rg_a, b_rg_a, w_rg_i,
                         b_rg_i, lru_lambda, w_rec_o, w_out, ln2_g, w_ff1, w_ff2, lnf_g)))
    m = dict(zip(NAMES, (m_ln1_g, m_w_in, m_b_in, m_rpb, m_w_att_o, m_conv_w, m_conv_b, m_w_rg_a,
                         m_b_rg_a, m_w_rg_i, m_b_rg_i, m_lru_lambda, m_w_rec_o, m_w_out, m_ln2_g,
                         m_w_ff1, m_w_ff2, m_lnf_g)))
    v = dict(zip(NAMES, (v_ln1_g, v_w_in, v_b_in, v_rpb, v_w_att_o, v_conv_w, v_conv_b, v_w_rg_a,
                         v_b_rg_a, v_w_rg_i, v_b_rg_i, v_lru_lambda, v_w_rec_o, v_w_out, v_ln2_g,
                         v_w_ff1, v_w_ff2, v_lnf_g)))
    xi, yi, ci = _position()

    shard = {t: w[n][0].T.astype(BF16) for n, t in TRANSPOSED.items()}
    shard.update({n: w[n][0].astype(BF16) for n in ROW_SHARDED})
    shard["chan"] = _chan_bits([w[n][0] for n, _ in CHAN])
    first, later = ("w_in_t", "chan"), ("w_rec_o", "w_out", "w_att_o_t", "w_ff1_t", "w_ff2")
    p = dict(zip(first, _all_gather([shard[n] for n in first], "weight_all_gather", (4, 1))))
    send_sems, recv_sems, sent, zones, token = _gather_start([shard[n] for n in later], p["w_in_t"],
                                                             "weight_gather_start")

    stages = (("w_rec_o", "w_out", "w_att_o_t"), ("w_ff1_t", "w_ff2"))
    passing = {}

    def late_weights(after, stage):
        which = [later.index(n) for n in stages[stage]]
        _, arrived = _gather_wait(
            send_sems, recv_sems, [shard[n].shape[0] for n in later], which, [sent[i] for i in which],
            [zones[i] for i in which], after, "weight_gather_wait_%d" % stage)
        passing[stage] = _pass_on_start(
            [shard[n].shape[0] for n in stages[stage]], arrived,
            PASS_ON_IDS[stage], "weight_pass_on_start_%d" % stage)
        return passing[stage][-1]

    def late_weights_ready(after, stage):
        pass_send_sems, pass_recv_sems, pass_zones, _ = passing[stage]
        return dict(zip(stages[stage], _pass_on_wait(
            [shard[n].shape[0] for n in stages[stage]], pass_send_sems, pass_recv_sems, pass_zones, after,
            "weight_pass_on_wait_%d" % stage)))

    chan = _chan_from_bits(p.pop("chan"))
    r0 = 0
    for n, rows in CHAN:
        p[n] = chan[r0:r0 + rows]
        r0 += rows
    p.update(ln1_g=w["ln1_g"], b_in=w["b_in"], later_weights_started=token, rpb=w["rpb"][0], conv_b=w["conv_b"],
             w_rg_a=w["w_rg_a"][0], w_rg_i=w["w_rg_i"][0], ln2_g=w["ln2_g"],
             lnf_g=w["lnf_g"].reshape(1, D))

    core = jnp.reshape(ci, (1,)).astype(jnp.int32)
    chip = jnp.reshape(2 * xi + yi, (1,)).astype(jnp.int32)
    first_sections = tuple(s for s in SECTIONS if s[0] in ("w_ff1_t", "w_ff2"))
    late_sections = SECTIONS[:1]
    early_sections = tuple(s for s in SECTIONS[1:] if s not in first_sections)
    in_flight = {}

    def pair_sum_and_send(group, sections, after):
        send_sems, recv_sems, sect, zones, _ = in_flight["pair_" + group]
        sect, got = _pair_exchange_wait(sections, send_sems, recv_sems, sect, zones, after,
                                        "grad_pair_exchange_wait_" + group)
        parts = _pair_add(sections, sect, got, core, "grad_pair_add_" + group)
        in_flight[group] = _chip_exchange_start(sections, parts, "grad_chip_exchange_start_" + group)
        return in_flight[group][-1]

    def pair_exchange_at_once(group, sections, grads, barrier_id):
        in_flight["pair_" + group] = _pair_exchange_start(
            sections, [grads[n] for n, _, _ in sections], barrier_id, "grad_pair_exchange_start_" + group)
        return pair_sum_and_send(group, sections, in_flight["pair_" + group][-1])

    def reduce_first(grads, after):
        if grads is None:
            return pair_sum_and_send("first", first_sections, after)
        in_flight["pair_first"] = _pair_exchange_start(
            first_sections, [grads[n] for n, _, _ in first_sections], PAIR_FIRST_ID,
            "grad_pair_exchange_start_first")
        return in_flight["pair_first"][-1]

    def reduce_early(grads):
        chan_g = jnp.concatenate([grads[n] for n, _ in CHAN], axis=0)
        chan_g = chan_g.reshape(N_CHAN_ROWS, N_DEV, LANES).transpose(1, 0, 2).astype(BF16)
        chan_g = jnp.pad(chan_g.reshape(N_DEV, -1), ((0, 0), (0, CHAN_BLOCK_ROWS * D - N_CHAN_ROWS * LANES)))
        grads["chan"] = chan_g.reshape(N_DEV * CHAN_BLOCK_ROWS, D)
        grads["gates"] = jnp.concatenate([grads[n].reshape(-1, D) for n in GATE_BLOCKS], axis=0).astype(BF16)
        return pair_exchange_at_once("early", early_sections, grads, PAIR_EARLY_ID)

    def finish(group, sections, after, name):
        send_sems, recv_sems, parts, zones, _ = in_flight[group]
        parts, far = _chip_exchange_wait(sections, send_sems, recv_sems, parts, zones, after,
                                         "grad_chip_exchange_wait_" + name)
        return dict(zip((n for n, _, _ in sections),
                        _grad_finish(sections, parts, far, chip, "grad_finish_" + name)))

    summed = {}

    def reduce_late(grads):
        in_flight["pair_late"] = _pair_exchange_start(
            late_sections, [grads[n] for n, _, _ in late_sections], PAIR_LATE_ID,
            "grad_pair_exchange_start_late")
        summed.update(finish("first", first_sections, in_flight["pair_late"][-1], "first"))
        summed.update(finish("early", early_sections, summed["w_ff2"], "early"))
        return pair_sum_and_send("late", late_sections, summed["gates"])

    loss_part, grad_x, grads = _local_step(x[0], loss_target[0], p, late_weights, late_weights_ready,
                                           reduce_first, reduce_early, reduce_late)

    flat = jnp.concatenate([grads[n].reshape(-1) for n, _ in REPLICATED if n not in GATE_BLOCKS]
                           + [loss_part.reshape(-1)])
    n_small = flat.shape[0]
    flat = jnp.pad(flat, (0, SMALL_ROWS * LANES - n_small)).reshape(SMALL_ROWS, LANES)
    *small_gather, small_started = _gather_start([flat, summed["gates"]], None, "small_grad_gather_start")

    g, delta, new_m, new_v = {}, {}, {}, {}

    def update(n, g2, shape2, after=None):
        d2, m2, v2 = _adamw(w[n].reshape(shape2), g2, m[n].reshape(shape2), v[n].reshape(shape2),
                            "adamw_" + n, after)
        g[n], delta[n], new_m[n], new_v[n] = (a.reshape(w[n].shape) for a in (g2, d2, m2, v2))

    for n in ROW_SHARDED:
        update(n, summed[n], summed[n].shape, small_started)
    for n, t in TRANSPOSED.items():
        if t in summed:
            update(n, summed[t].T, summed[t].shape[::-1], small_started)

    small_rows = [SMALL_ROWS, GATE_ROWS]
    _, small_zones = _gather_wait(
        *small_gather[:2], small_rows, range(2), *small_gather[2:],
        _after_all(list(delta.values()), "sharded_updates_done"), "small_grad_gather_wait")
    small_parts, gate_sum = _gather_pass_on(small_rows, small_zones, SMALL_PASS_ON_ID,
                                            "small_grad_gather_pass_on")
    small = _sum_devices(small_parts, SMALL_ROWS, "small_grad_sum").reshape(-1)
    loss = small[n_small - 1]

    small_params = []
    o = 0
    for n, shape2 in REPLICATED:
        if n in GATE_BLOCKS:
            k, rows = GATE_BLOCKS.index(n), gate_sum.shape[0] // len(GATE_BLOCKS)
            update(n, gate_sum[k * rows:(k + 1) * rows].reshape(shape2), shape2)
        else:
            size = shape2[0] * shape2[1]
            small_params.append((n, small[o:o + size].reshape(shape2), shape2))
            o += size
    chan_back = summed["chan"].reshape(-1)[:N_CHAN_ROWS * LANES].reshape(N_CHAN_ROWS, LANES)
    r0 = 0
    for n, rows in CHAN:
        small_params.append((n, chan_back[r0:r0 + rows], (rows, LANES)))
        r0 += rows
    results = _adamw_small([(w[n].reshape(s2), g2, m[n].reshape(s2), v[n].reshape(s2))
                            for n, g2, s2 in small_params], "adamw_vectors")
    for (n, g2, _), (d2, m2, v2) in zip(small_params, results):
        g[n], delta[n], new_m[n], new_v[n] = (a.reshape(w[n].shape) for a in (g2, d2, m2, v2))

    summed = finish("late", late_sections, _after_all(list(delta.values()), "updates_done"), "late")
    g_t = summed["w_in_t"]
    results = _adamw(w["w_in"][0].T, g_t, m["w_in"][0].T, v["w_in"][0].T, "adamw_w_in")
    g["w_in"], delta["w_in"], new_m["w_in"], new_v["w_in"] = (a.T[None] for a in (g_t, *results))

    return (loss, grad_x[None], *[g[n] for n in NAMES], *[delta[n] for n in NAMES],
            *[new_m[n] for n in NAMES], *[new_v[n] for n in NAMES])
```
